```python
import jax, jax.numpy as jnp
from jax import lax
import numpy as np

D_MODEL = 1024
BATCH = 8
SEQ = 2048
DEPTH = 1

GRID_W = 64
CTX_LEN = 256
N_MOD = 6
HG_HEADS = 4
HG_DK = 128
HG_DV = 128
GLA_HEADS = 4
GLA_DK = 128
GLA_DV = 128
GLA_RANK = 16
GLA_GATE_NORM = 16.0
HG_W = HG_HEADS * HG_DK
HG_VW = HG_HEADS * HG_DV
GLA_KW = GLA_HEADS * GLA_DK
GLA_VW = GLA_HEADS * GLA_DV
D_FF = ((8 * D_MODEL // 3 + 255) // 256) * 256
EPS = 1e-6
IN_SPLITS = (HG_W, HG_VW, HG_W, HG_W, HG_VW, GLA_KW, GLA_KW, GLA_VW, GLA_VW, GLA_RANK, GLA_RANK, D_MODEL, D_MODEL)
IN_WIDTH = sum(IN_SPLITS)
IN_OFFSETS = tuple(int(v) for v in np.cumsum(IN_SPLITS)[:-1])

kernel_name = "hybrid_hgrn2_gla_prefix_dit_block"


def rms_norm(a, w):
    af = a.astype(jnp.float32)
    return (af * lax.rsqrt(jnp.mean(af * af, axis=-1, keepdims=True) + EPS)).astype(a.dtype) * w


def modulate(h, shift_c, scale_c, shift_x, scale_x):
    hc, hx = h[:, :CTX_LEN], h[:, CTX_LEN:]
    return jnp.concatenate([hc * (1 + scale_c) + shift_c,
                            hx * (1 + scale_x[:, None]) + shift_x[:, None]], axis=1)


def apply_gate(h, gate_c, gate_x):
    return jnp.concatenate([h[:, :CTX_LEN] * gate_c, h[:, CTX_LEN:] * gate_x[:, None]], axis=1)


def to_heads(a, n_heads):
    b, t, w = a.shape
    return a.reshape(b, t, n_heads, w // n_heads).transpose(0, 2, 1, 3)


def merge_heads(a):
    b, h, t, d = a.shape
    return a.transpose(0, 2, 1, 3).reshape(b, t, h * d)


def segment_reverse(a):
    return jnp.concatenate([jnp.flip(a[:, :, :CTX_LEN], axis=2), jnp.flip(a[:, :, CTX_LEN:], axis=2)], axis=2)


def chunk_scan(q, k, v, log_f, n_chunks):
    b, h, t, dk = q.shape
    dv = v.shape[-1]

    def chunks(a):
        return jnp.moveaxis(a.reshape(b, h, n_chunks, GRID_W, a.shape[-1]), 2, 0)

    causal = jnp.tril(jnp.ones((GRID_W, GRID_W), dtype=bool))[:, :, None]

    def step(s, blk):
        qc, kc, vc, gc = blk
        cum = jnp.cumsum(gc.astype(jnp.float32), axis=2)
        pair = jnp.exp(jnp.where(causal, cum[:, :, :, None, :] - cum[:, :, None, :, :], -jnp.inf))
        scores = jnp.einsum('bhtk,bhtsk,bhsk->bhts', qc, pair, kc)
        o = (jnp.einsum('bhts,bhsv->bhtv', scores, vc)
             + jnp.einsum('bhtk,bhkv->bhtv', qc * jnp.exp(cum), s))
        tail = jnp.exp(cum[:, :, -1:, :] - cum)
        s_new = (s * jnp.exp(cum[:, :, -1, :])[..., None]
                 + jnp.einsum('bhsk,bhsv->bhkv', kc * tail, vc))
        return s_new, o

    s0 = jnp.zeros((b, h, dk, dv), jnp.float32)
    _, o = lax.scan(step, s0, (chunks(q), chunks(k), chunks(v), chunks(log_f)))
    return jnp.moveaxis(o, 0, 2).reshape(b, h, t, dv).astype(v.dtype)


def bidirectional_scan(q, k_fw, k_bw, v, lf_fw, lf_bw, n_chunks):
    o_fw = chunk_scan(q, k_fw, v, lf_fw, n_chunks)
    o_bw = chunk_scan(segment_reverse(q), segment_reverse(k_bw), segment_reverse(v),
                      segment_reverse(lf_bw), n_chunks)
    return o_fw + segment_reverse(o_bw)


def hybrid_mixer(h, w_in, lb, hg_onorm, gla_w_gk, gla_b_gk, gla_onorm, w_br_hg, w_br_gla, w_out, n_chunks):
    (hq, hi, hf_fw, hf_bw, hg_gate, gq, gk, gv, g_gate,
     lr_fw, lr_bw, gate_hg, gate_gla) = jnp.split(h @ w_in, IN_OFFSETS, axis=-1)

    def hg_forget(raw, lb_dir):
        f = lb_dir + (1 - lb_dir) * jax.nn.sigmoid(raw.astype(jnp.float32))
        return to_heads(1 - f, HG_HEADS), to_heads(jnp.log(f), HG_HEADS)

    q = to_heads(jax.nn.silu(hq), HG_HEADS)
    i = to_heads(hi, HG_HEADS)
    k_fw, lf_fw = hg_forget(hf_fw, lb[0])
    k_bw, lf_bw = hg_forget(hf_bw, lb[1])
    o = bidirectional_scan(q, k_fw, k_bw, i, lf_fw, lf_bw, n_chunks)
    o_hg = merge_heads(rms_norm(o, hg_onorm)) * jax.nn.silu(hg_gate)

    def gla_gate_log(lr, w, bias):
        return to_heads(jax.nn.log_sigmoid((lr @ w + bias).astype(jnp.float32)) / GLA_GATE_NORM, GLA_HEADS)

    q = to_heads(gq, GLA_HEADS) * GLA_DK ** -0.5
    k = to_heads(gk, GLA_HEADS)
    v = to_heads(gv, GLA_HEADS)
    lf_fw = gla_gate_log(lr_fw, gla_w_gk[0], gla_b_gk[0])
    lf_bw = gla_gate_log(lr_bw, gla_w_gk[1], gla_b_gk[1])
    o = bidirectional_scan(q, k, k, v, lf_fw, lf_bw, n_chunks)
    o_gla = merge_heads(rms_norm(o, gla_onorm)) * jax.nn.silu(g_gate)

    merged = (jax.nn.sigmoid(gate_hg) * (o_hg @ w_br_hg)
              + jax.nn.sigmoid(gate_gla) * (o_gla @ w_br_gla))
    return merged @ w_out


def swiglu(h, w_gate, w_up, w_down):
    return (jax.nn.silu(h @ w_gate) * (h @ w_up)) @ w_down


def _fwd_setup_inputs(seed: int = 0) -> dict:
    key = jax.random.key(seed)
    ks = jax.random.split(key, 24)

    def nrm(k, shape, scale):
        return jax.random.normal(k, shape, jnp.float32) * scale

    def gain(k, shape):
        return 1.0 + nrm(k, shape, 0.05)

    return {
        "x": nrm(ks[0], (BATCH, SEQ, D_MODEL), 1.0),
        "c": nrm(ks[1], (BATCH, D_MODEL), 1.0),
        "ctx": nrm(ks[2], (BATCH, CTX_LEN, D_MODEL), 1.0),
        "c_ctx": nrm(ks[3], (D_MODEL,), 1.0),
        "w_mod": nrm(ks[4], (DEPTH, D_MODEL, N_MOD * D_MODEL), 0.5 * D_MODEL ** -0.5),
        "b_mod": nrm(ks[5], (DEPTH, N_MOD * D_MODEL), 0.01),
        "norm_pre1": gain(ks[6], (DEPTH, D_MODEL)),
        "norm_post1": gain(ks[7], (DEPTH, D_MODEL)),
        "norm_pre2": gain(ks[8], (DEPTH, D_MODEL)),
        "norm_post2": gain(ks[9], (DEPTH, D_MODEL)),
        "w_in": nrm(ks[10], (DEPTH, D_MODEL, IN_WIDTH), D_MODEL ** -0.5),
        "hg_lb": nrm(ks[11], (DEPTH + 1, 2, HG_W), 1.0),
        "hg_onorm": gain(ks[12], (DEPTH, HG_DV)),
        "gla_w_gk": nrm(ks[13], (DEPTH, 2, GLA_RANK, GLA_KW), GLA_RANK ** -0.5),
        "gla_b_gk": nrm(ks[14], (DEPTH, 2, GLA_KW), 0.1),
        "gla_onorm": gain(ks[15], (DEPTH, GLA_DV)),
        "w_br_hg": nrm(ks[16], (DEPTH, HG_VW, D_MODEL), HG_VW ** -0.5),
        "w_br_gla": nrm(ks[17], (DEPTH, GLA_VW, D_MODEL), GLA_VW ** -0.5),
        "w_out": nrm(ks[18], (DEPTH, D_MODEL, D_MODEL), D_MODEL ** -0.5),
        "w_ff_gate": nrm(ks[19], (DEPTH, D_MODEL, D_FF), D_MODEL ** -0.5),
        "w_ff_up": nrm(ks[20], (DEPTH, D_MODEL, D_FF), D_MODEL ** -0.5),
        "w_ff_down": nrm(ks[21], (DEPTH, D_FF, D_MODEL), D_FF ** -0.5),
    }


def _fwd_reference(x, c, ctx, c_ctx, w_mod, b_mod, norm_pre1, norm_post1, norm_pre2, norm_post2, w_in, hg_lb,
              hg_onorm, gla_w_gk, gla_b_gk, gla_onorm, w_br_hg, w_br_gla, w_out, w_ff_gate, w_ff_up, w_ff_down):
    rows = x.shape[1] // GRID_W
    n_chunks = CTX_LEN // GRID_W + rows
    z = jnp.concatenate([ctx, x], axis=1)
    lb_all = jnp.cumsum(jax.nn.softmax(hg_lb.astype(jnp.float32), axis=0), axis=0)
    for l in range(DEPTH):
        m_c = jnp.split(jax.nn.silu(c_ctx) @ w_mod[l] + b_mod[l], N_MOD, axis=-1)
        m_x = jnp.split(jax.nn.silu(c) @ w_mod[l] + b_mod[l], N_MOD, axis=-1)
        h = modulate(rms_norm(z, norm_pre1[l]), m_c[0], m_c[1], m_x[0], m_x[1])
        y = hybrid_mixer(h, w_in[l], lb_all[l], hg_onorm[l], gla_w_gk[l], gla_b_gk[l], gla_onorm[l],
                         w_br_hg[l], w_br_gla[l], w_out[l], n_chunks)
        z = z + apply_gate(rms_norm(y, norm_post1[l]), m_c[2], m_x[2])
        h = modulate(rms_norm(z, norm_pre2[l]), m_c[3], m_c[4], m_x[3], m_x[4])
        y = swiglu(h, w_ff_gate[l], w_ff_up[l], w_ff_down[l])
        z = z + apply_gate(rms_norm(y, norm_post2[l]), m_c[5], m_x[5])
    return z[:, CTX_LEN:]


import jax as _jax
import jax.numpy as _jnp

TWIN_FORMAT = 'train_step'
FWD_PARAMS = ['x', 'c', 'ctx', 'c_ctx', 'w_mod', 'b_mod', 'norm_pre1', 'norm_post1', 'norm_pre2', 'norm_post2', 'w_in', 'hg_lb', 'hg_onorm', 'gla_w_gk', 'gla_b_gk', 'gla_onorm', 'w_br_hg', 'w_br_gla', 'w_out', 'w_ff_gate', 'w_ff_up', 'w_ff_down']
TWIN_WEIGHTS = ['c_ctx', 'w_mod', 'b_mod', 'norm_pre1', 'norm_post1', 'norm_pre2', 'norm_post2', 'w_in', 'hg_lb', 'hg_onorm', 'gla_w_gk', 'gla_b_gk', 'gla_onorm', 'w_br_hg', 'w_br_gla', 'w_out', 'w_ff_gate', 'w_ff_up', 'w_ff_down']
TWIN_DIFF_INPUT = 'x'
TWIN_INPUTS = ['x', 'c', 'ctx', 'c_ctx', 'w_mod', 'b_mod', 'norm_pre1', 'norm_post1', 'norm_pre2', 'norm_post2', 'w_in', 'hg_lb', 'hg_onorm', 'gla_w_gk', 'gla_b_gk', 'gla_onorm', 'w_br_hg', 'w_br_gla', 'w_out', 'w_ff_gate', 'w_ff_up', 'w_ff_down', 'loss_target', 'm_c_ctx', 'm_w_mod', 'm_b_mod', 'm_norm_pre1', 'm_norm_post1', 'm_norm_pre2', 'm_norm_post2', 'm_w_in', 'm_hg_lb', 'm_hg_onorm', 'm_gla_w_gk', 'm_gla_b_gk', 'm_gla_onorm', 'm_w_br_hg', 'm_w_br_gla', 'm_w_out', 'm_w_ff_gate', 'm_w_ff_up', 'm_w_ff_down', 'v_c_ctx', 'v_w_mod', 'v_b_mod', 'v_norm_pre1', 'v_norm_post1', 'v_norm_pre2', 'v_norm_post2', 'v_w_in', 'v_hg_lb', 'v_hg_onorm', 'v_gla_w_gk', 'v_gla_b_gk', 'v_gla_onorm', 'v_w_br_hg', 'v_w_br_gla', 'v_w_out', 'v_w_ff_gate', 'v_w_ff_up', 'v_w_ff_down']
TWIN_OUTPUTS = ['loss', 'grad_x', 'grad_c_ctx', 'grad_w_mod', 'grad_b_mod', 'grad_norm_pre1', 'grad_norm_post1', 'grad_norm_pre2', 'grad_norm_post2', 'grad_w_in', 'grad_hg_lb', 'grad_hg_onorm', 'grad_gla_w_gk', 'grad_gla_b_gk', 'grad_gla_onorm', 'grad_w_br_hg', 'grad_w_br_gla', 'grad_w_out', 'grad_w_ff_gate', 'grad_w_ff_up', 'grad_w_ff_down', 'delta_c_ctx', 'delta_w_mod', 'delta_b_mod', 'delta_norm_pre1', 'delta_norm_post1', 'delta_norm_pre2', 'delta_norm_post2', 'delta_w_in', 'delta_hg_lb', 'delta_hg_onorm', 'delta_gla_w_gk', 'delta_gla_b_gk', 'delta_gla_onorm', 'delta_w_br_hg', 'delta_w_br_gla', 'delta_w_out', 'delta_w_ff_gate', 'delta_w_ff_up', 'delta_w_ff_down', 'new_m_c_ctx', 'new_m_w_mod', 'new_m_b_mod', 'new_m_norm_pre1', 'new_m_norm_post1', 'new_m_norm_pre2', 'new_m_norm_post2', 'new_m_w_in', 'new_m_hg_lb', 'new_m_hg_onorm', 'new_m_gla_w_gk', 'new_m_gla_b_gk', 'new_m_gla_onorm', 'new_m_w_br_hg', 'new_m_w_br_gla', 'new_m_w_out', 'new_m_w_ff_gate', 'new_m_w_ff_up', 'new_m_w_ff_down', 'new_v_c_ctx', 'new_v_w_mod', 'new_v_b_mod', 'new_v_norm_pre1', 'new_v_norm_post1', 'new_v_norm_pre2', 'new_v_norm_post2', 'new_v_w_in', 'new_v_hg_lb', 'new_v_hg_onorm', 'new_v_gla_w_gk', 'new_v_gla_b_gk', 'new_v_gla_onorm', 'new_v_w_br_hg', 'new_v_w_br_gla', 'new_v_w_out', 'new_v_w_ff_gate', 'new_v_w_ff_up', 'new_v_w_ff_down']
TWIN_LEAF_KINDS = {'loss': 'loss', 'grad_x': 'grad_x', 'grad_c_ctx': 'grad_w', 'grad_w_mod': 'grad_w', 'grad_b_mod': 'grad_w', 'grad_norm_pre1': 'grad_w', 'grad_norm_post1': 'grad_w', 'grad_norm_pre2': 'grad_w', 'grad_norm_post2': 'grad_w', 'grad_w_in': 'grad_w', 'grad_hg_lb': 'grad_w', 'grad_hg_onorm': 'grad_w', 'grad_gla_w_gk': 'grad_w', 'grad_gla_b_gk': 'grad_w', 'grad_gla_onorm': 'grad_w', 'grad_w_br_hg': 'grad_w', 'grad_w_br_gla': 'grad_w', 'grad_w_out': 'grad_w', 'grad_w_ff_gate': 'grad_w', 'grad_w_ff_up': 'grad_w', 'grad_w_ff_down': 'grad_w', 'delta_c_ctx': 'delta_w', 'delta_w_mod': 'delta_w', 'delta_b_mod': 'delta_w', 'delta_norm_pre1': 'delta_w', 'delta_norm_post1': 'delta_w', 'delta_norm_pre2': 'delta_w', 'delta_norm_post2': 'delta_w', 'delta_w_in': 'delta_w', 'delta_hg_lb': 'delta_w', 'delta_hg_onorm': 'delta_w', 'delta_gla_w_gk': 'delta_w', 'delta_gla_b_gk': 'delta_w', 'delta_gla_onorm': 'delta_w', 'delta_w_br_hg': 'delta_w', 'delta_w_br_gla': 'delta_w', 'delta_w_out': 'delta_w', 'delta_w_ff_gate': 'delta_w', 'delta_w_ff_up': 'delta_w', 'delta_w_ff_down': 'delta_w', 'new_m_c_ctx': 'new_m', 'new_m_w_mod': 'new_m', 'new_m_b_mod': 'new_m', 'new_m_norm_pre1': 'new_m', 'new_m_norm_post1': 'new_m', 'new_m_norm_pre2': 'new_m', 'new_m_norm_post2': 'new_m', 'new_m_w_in': 'new_m', 'new_m_hg_lb': 'new_m', 'new_m_hg_onorm': 'new_m', 'new_m_gla_w_gk': 'new_m', 'new_m_gla_b_gk': 'new_m', 'new_m_gla_onorm': 'new_m', 'new_m_w_br_hg': 'new_m', 'new_m_w_br_gla': 'new_m', 'new_m_w_out': 'new_m', 'new_m_w_ff_gate': 'new_m', 'new_m_w_ff_up': 'new_m', 'new_m_w_ff_down': 'new_m', 'new_v_c_ctx': 'new_v', 'new_v_w_mod': 'new_v', 'new_v_b_mod': 'new_v', 'new_v_norm_pre1': 'new_v', 'new_v_norm_post1': 'new_v', 'new_v_norm_pre2': 'new_v', 'new_v_norm_post2': 'new_v', 'new_v_w_in': 'new_v', 'new_v_hg_lb': 'new_v', 'new_v_hg_onorm': 'new_v', 'new_v_gla_w_gk': 'new_v', 'new_v_gla_b_gk': 'new_v', 'new_v_gla_onorm': 'new_v', 'new_v_w_br_hg': 'new_v', 'new_v_w_br_gla': 'new_v', 'new_v_w_out': 'new_v', 'new_v_w_ff_gate': 'new_v', 'new_v_w_ff_up': 'new_v', 'new_v_w_ff_down': 'new_v'}


def _forward(args):
    return _fwd_reference(*[args[k] for k in FWD_PARAMS])


def _output_shape():
    out = _jax.eval_shape(lambda: _forward(_fwd_setup_inputs(0)))
    return out.shape, out.dtype

N_MICROBATCH = 1
ADAM_LR = 0.001
ADAM_B1 = 0.9
ADAM_B2 = 0.999
ADAM_EPS = 1e-08
ADAM_WD = 0.01
ADAM_STEP = 10
PER_EXAMPLE_BATCH_AXIS = {'x': 0, 'c': 0, 'ctx': 0, 'loss_target': 0}
SHARED_INPUTS = []
_WEIGHT_DTYPES = {'c_ctx': _jnp.float32, 'w_mod': _jnp.float32, 'b_mod': _jnp.float32, 'norm_pre1': _jnp.float32, 'norm_post1': _jnp.float32, 'norm_pre2': _jnp.float32, 'norm_post2': _jnp.float32, 'w_in': _jnp.float32, 'hg_lb': _jnp.float32, 'hg_onorm': _jnp.float32, 'gla_w_gk': _jnp.float32, 'gla_b_gk': _jnp.float32, 'gla_onorm': _jnp.float32, 'w_br_hg': _jnp.float32, 'w_br_gla': _jnp.float32, 'w_out': _jnp.float32, 'w_ff_gate': _jnp.float32, 'w_ff_up': _jnp.float32, 'w_ff_down': _jnp.float32}
MOMENT_SCALE = {'c_ctx': 7.567115e-03, 'w_mod': 8.884431e-01, 'b_mod': 1.659140e+00, 'norm_pre1': 1.102945e-01, 'norm_post1': 1.883750e+00, 'norm_pre2': 8.844008e-02, 'norm_post2': 1.936914e+00, 'w_in': 5.096897e-02, 'hg_lb': 2.696099e-03, 'hg_onorm': 1.813014e-01, 'gla_w_gk': 8.202056e-03, 'gla_b_gk': 2.284071e-02, 'gla_onorm': 1.495879e-01, 'w_br_hg': 6.318068e-02, 'w_br_gla': 5.028195e-02, 'w_out': 8.440711e-02, 'w_ff_gate': 3.972795e-02, 'w_ff_up': 4.191955e-02, 'w_ff_down': 7.182272e-02}


def _to_microbatches(a, axis):
    t = _jnp.moveaxis(a, axis, 0)
    t = t.reshape((N_MICROBATCH, t.shape[0] // N_MICROBATCH) + t.shape[1:])
    return _jnp.moveaxis(t, 1, axis + 1)


def setup_inputs(seed: int = 0) -> dict:
    inp = _fwd_setup_inputs(seed)
    key = _jax.random.fold_in(_jax.random.key(seed), 7919)
    shape, _ = _output_shape()
    out = dict(inp)
    out["loss_target"] = _jax.random.normal(_jax.random.fold_in(key, 0), shape, _jnp.float32)
    for i, name in enumerate(TWIN_WEIGHTS):
        w = inp[name].astype(_jnp.float32)
        if MOMENT_SCALE is None:
            s = _jnp.sqrt(_jnp.mean(_jnp.square(w)) + 1e-30)
        else:
            s = MOMENT_SCALE[name]
        km, kv = _jax.random.split(_jax.random.fold_in(key, i + 1))
        out[name] = w
        out["m_" + name] = s * _jax.random.normal(km, w.shape, _jnp.float32)
        out["v_" + name] = (s * s) * _jax.random.uniform(kv, w.shape, _jnp.float32, 0.5, 1.5)
    if N_MICROBATCH > 1:
        for name, axis in PER_EXAMPLE_BATCH_AXIS.items():
            out[name] = _to_microbatches(out[name], axis)
    return {'x': out['x'], 'c': out['c'], 'ctx': out['ctx'], 'c_ctx': out['c_ctx'], 'w_mod': out['w_mod'], 'b_mod': out['b_mod'], 'norm_pre1': out['norm_pre1'], 'norm_post1': out['norm_post1'], 'norm_pre2': out['norm_pre2'], 'norm_post2': out['norm_post2'], 'w_in': out['w_in'], 'hg_lb': out['hg_lb'], 'hg_onorm': out['hg_onorm'], 'gla_w_gk': out['gla_w_gk'], 'gla_b_gk': out['gla_b_gk'], 'gla_onorm': out['gla_onorm'], 'w_br_hg': out['w_br_hg'], 'w_br_gla': out['w_br_gla'], 'w_out': out['w_out'], 'w_ff_gate': out['w_ff_gate'], 'w_ff_up': out['w_ff_up'], 'w_ff_down': out['w_ff_down'], 'loss_target': out['loss_target'], 'm_c_ctx': out['m_c_ctx'], 'm_w_mod': out['m_w_mod'], 'm_b_mod': out['m_b_mod'], 'm_norm_pre1': out['m_norm_pre1'], 'm_norm_post1': out['m_norm_post1'], 'm_norm_pre2': out['m_norm_pre2'], 'm_norm_post2': out['m_norm_post2'], 'm_w_in': out['m_w_in'], 'm_hg_lb': out['m_hg_lb'], 'm_hg_onorm': out['m_hg_onorm'], 'm_gla_w_gk': out['m_gla_w_gk'], 'm_gla_b_gk': out['m_gla_b_gk'], 'm_gla_onorm': out['m_gla_onorm'], 'm_w_br_hg': out['m_w_br_hg'], 'm_w_br_gla': out['m_w_br_gla'], 'm_w_out': out['m_w_out'], 'm_w_ff_gate': out['m_w_ff_gate'], 'm_w_ff_up': out['m_w_ff_up'], 'm_w_ff_down': out['m_w_ff_down'], 'v_c_ctx': out['v_c_ctx'], 'v_w_mod': out['v_w_mod'], 'v_b_mod': out['v_b_mod'], 'v_norm_pre1': out['v_norm_pre1'], 'v_norm_post1': out['v_norm_post1'], 'v_norm_pre2': out['v_norm_pre2'], 'v_norm_post2': out['v_norm_post2'], 'v_w_in': out['v_w_in'], 'v_hg_lb': out['v_hg_lb'], 'v_hg_onorm': out['v_hg_onorm'], 'v_gla_w_gk': out['v_gla_w_gk'], 'v_gla_b_gk': out['v_gla_b_gk'], 'v_gla_onorm': out['v_gla_onorm'], 'v_w_br_hg': out['v_w_br_hg'], 'v_w_br_gla': out['v_w_br_gla'], 'v_w_out': out['v_w_out'], 'v_w_ff_gate': out['v_w_ff_gate'], 'v_w_ff_up': out['v_w_ff_up'], 'v_w_ff_down': out['v_w_ff_down']}


def _loss(weights, diff, rest, loss_target):
    with _jax.named_scope("forward"):
        args = {**rest, TWIN_DIFF_INPUT: diff, **{k: w.astype(_WEIGHT_DTYPES[k]) for k, w in weights.items()}}
        y = _forward(args)
    with _jax.named_scope("loss_head"):
        err = _jnp.square(y.astype(_jnp.float32) - loss_target)
        return 0.5 * _jnp.sum(_jnp.mean(err, axis=-1)) if err.ndim else 0.5 * err


def _adamw(w, g, m, v):
    m = ADAM_B1 * m + (1.0 - ADAM_B1) * g
    v = ADAM_B2 * v + (1.0 - ADAM_B2) * _jnp.square(g)
    m_hat = m / (1.0 - ADAM_B1 ** ADAM_STEP)
    v_hat = v / (1.0 - ADAM_B2 ** ADAM_STEP)
    delta = -ADAM_LR * (m_hat / (_jnp.sqrt(v_hat) + ADAM_EPS) + ADAM_WD * w)
    return delta, m, v


def reference(x, c, ctx, c_ctx, w_mod, b_mod, norm_pre1, norm_post1, norm_pre2, norm_post2, w_in, hg_lb, hg_onorm, gla_w_gk, gla_b_gk, gla_onorm, w_br_hg, w_br_gla, w_out, w_ff_gate, w_ff_up, w_ff_down, loss_target, m_c_ctx, m_w_mod, m_b_mod, m_norm_pre1, m_norm_post1, m_norm_pre2, m_norm_post2, m_w_in, m_hg_lb, m_hg_onorm, m_gla_w_gk, m_gla_b_gk, m_gla_onorm, m_w_br_hg, m_w_br_gla, m_w_out, m_w_ff_gate, m_w_ff_up, m_w_ff_down, v_c_ctx, v_w_mod, v_b_mod, v_norm_pre1, v_norm_post1, v_norm_pre2, v_norm_post2, v_w_in, v_hg_lb, v_hg_onorm, v_gla_w_gk, v_gla_b_gk, v_gla_onorm, v_w_br_hg, v_w_br_gla, v_w_out, v_w_ff_gate, v_w_ff_up, v_w_ff_down):
    given = dict(x=x, c=c, ctx=ctx, c_ctx=c_ctx, w_mod=w_mod, b_mod=b_mod, norm_pre1=norm_pre1, norm_post1=norm_post1, norm_pre2=norm_pre2, norm_post2=norm_post2, w_in=w_in, hg_lb=hg_lb, hg_onorm=hg_onorm, gla_w_gk=gla_w_gk, gla_b_gk=gla_b_gk, gla_onorm=gla_onorm, w_br_hg=w_br_hg, w_br_gla=w_br_gla, w_out=w_out, w_ff_gate=w_ff_gate, w_ff_up=w_ff_up, w_ff_down=w_ff_down, loss_target=loss_target, m_c_ctx=m_c_ctx, m_w_mod=m_w_mod, m_b_mod=m_b_mod, m_norm_pre1=m_norm_pre1, m_norm_post1=m_norm_post1, m_norm_pre2=m_norm_pre2, m_norm_post2=m_norm_post2, m_w_in=m_w_in, m_hg_lb=m_hg_lb, m_hg_onorm=m_hg_onorm, m_gla_w_gk=m_gla_w_gk, m_gla_b_gk=m_gla_b_gk, m_gla_onorm=m_gla_onorm, m_w_br_hg=m_w_br_hg, m_w_br_gla=m_w_br_gla, m_w_out=m_w_out, m_w_ff_gate=m_w_ff_gate, m_w_ff_up=m_w_ff_up, m_w_ff_down=m_w_ff_down, v_c_ctx=v_c_ctx, v_w_mod=v_w_mod, v_b_mod=v_b_mod, v_norm_pre1=v_norm_pre1, v_norm_post1=v_norm_post1, v_norm_pre2=v_norm_pre2, v_norm_post2=v_norm_post2, v_w_in=v_w_in, v_hg_lb=v_hg_lb, v_hg_onorm=v_hg_onorm, v_gla_w_gk=v_gla_w_gk, v_gla_b_gk=v_gla_b_gk, v_gla_onorm=v_gla_onorm, v_w_br_hg=v_w_br_hg, v_w_br_gla=v_w_br_gla, v_w_out=v_w_out, v_w_ff_gate=v_w_ff_gate, v_w_ff_up=v_w_ff_up, v_w_ff_down=v_w_ff_down)
    weights = {n: given[n] for n in TWIN_WEIGHTS}
    shared = {n: given[n] for n in SHARED_INPUTS}
    per_example = {n: given[n] for n in ['x', 'c', 'ctx']}
    grad_fn = _jax.value_and_grad(_loss, argnums=(0, 1))

    def one_microbatch(ex, loss_target):
        ex = dict(ex)
        diff = ex.pop(TWIN_DIFF_INPUT)
        return grad_fn(weights, diff, {**shared, **ex}, loss_target)

    if N_MICROBATCH == 1:
        loss, (grad_w, grad_x) = one_microbatch(per_example, given["loss_target"])
    else:
        def body(carry, xs):
            loss_sum, grad_sum = carry
            l_k, (gw_k, gx_k) = one_microbatch(xs[0], xs[1])
            with _jax.named_scope("update"):
                return (loss_sum + l_k, _jax.tree.map(_jnp.add, grad_sum, gw_k)), gx_k

        init = (_jnp.zeros((), _jnp.float32), _jax.tree.map(_jnp.zeros_like, weights))
        (loss, grad_w), grad_x = _jax.lax.scan(body, init, (per_example, given["loss_target"]))
    with _jax.named_scope("update"):
        delta_w, new_m, new_v = {}, {}, {}
        for n in TWIN_WEIGHTS:
            delta_w[n], new_m[n], new_v[n] = _adamw(weights[n], grad_w[n], given["m_" + n], given["v_" + n])
    return (loss, grad_x, *[grad_w[n] for n in TWIN_WEIGHTS], *[delta_w[n] for n in TWIN_WEIGHTS],
            *[new_m[n] for n in TWIN_WEIGHTS], *[new_v[n] for n in TWIN_WEIGHTS])
```

```python
import functools

import jax
import jax.numpy as jnp
from jax import lax
from jax.experimental import pallas as pl
from jax.experimental.pallas import tpu as pltpu

F32 = jnp.float32
BF16 = jnp.bfloat16
HI = lax.Precision.HIGHEST

N_DEV = 8
D = 1024
CTX = 256
HW = 512
DH = 128
NH = 8
D_FF = 2816
EPS = 1e-6
GLA_NORM = 16.0
CHUNK = 64
TR = 256
NCT = CTX // TR
W_IN_COLS = 7168
MAIN0 = 2048
LR0 = 6656
LEVELS = (32, 16, 8)
EXP_CLAMP = 80.0
VMEM_LIMIT = 48 * 1024 * 1024

ADAM_LR, ADAM_B1, ADAM_B2, ADAM_EPS, ADAM_WD, ADAM_STEP = 0.001, 0.9, 0.999, 1e-08, 0.01, 10


def _cp(*sem):
    return pltpu.CompilerParams(dimension_semantics=sem, vmem_limit_bytes=VMEM_LIMIT)


def _sig(x):
    return jax.nn.sigmoid(x)


def _silu(x):
    return x * _sig(x)


def _dsilu(x):
    s = _sig(x)
    return s * (1.0 + x * (1.0 - s))


def _rstd(x):
    return lax.rsqrt(jnp.mean(x * x, axis=-1, keepdims=True) + EPS)


def _rms_bwd(a, y, r):
    return r * (a - y * (r * r) * jnp.mean(a * y, axis=-1, keepdims=True))


def _colsum(x):
    return jnp.sum(x, axis=0, keepdims=True)


def _dot(a, b, dims, precision=None):
    return lax.dot_general(a, b, (dims, ((), ())), preferred_element_type=F32, precision=precision)


NN = ((1,), (0,))
NT = ((1,), (1,))
TN = ((0,), (0,))

SCAN_FWD_DT = BF16
SCAN_BWD_DT = F32


def _prec_dot(dt, a, b, dims, precision=None):
    if precision is None and dt == F32:
        precision = HI
    return _dot(a, b, dims, precision)


def _matmul(a, b, dims, out_dtype, name, tm, tn, tk, a_off=0, m_out=None):
    if dims == NN:
        m, k = a.shape[0], b.shape[0]
        n = b.shape[1]
        a_spec = pl.BlockSpec((tm, tk), lambda i, j, kk: (i, kk + a_off))
        b_spec = pl.BlockSpec((tk, tn), lambda i, j, kk: (kk, j))
    elif dims == NT:
        m, k = a.shape[0], b.shape[1]
        n = b.shape[0]
        a_spec = pl.BlockSpec((tm, tk), lambda i, j, kk: (i, kk + a_off))
        b_spec = pl.BlockSpec((tn, tk), lambda i, j, kk: (j, kk))
    else:
        m, k = (a.shape[1] if m_out is None else m_out), a.shape[0]
        n = b.shape[1]
        a_spec = pl.BlockSpec((tk, tm), lambda i, j, kk: (kk, i + a_off))
        b_spec = pl.BlockSpec((tk, tn), lambda i, j, kk: (kk, j))
    assert m % tm == 0 and n % tn == 0 and k % tk == 0, (name, m, n, k, tm, tn, tk)
    nk = k // tk

    def body(a_ref, b_ref, o_ref, *acc):
        part = _dot(a_ref[...], b_ref[...], dims)
        if nk == 1:
            o_ref[...] = part.astype(o_ref.dtype)
            return
        acc_ref, = acc
        kk = pl.program_id(2)

        @pl.when(kk == 0)
        def _():
            acc_ref[...] = part

        @pl.when(kk > 0)
        def _():
            acc_ref[...] += part

        @pl.when(kk == nk - 1)
        def _():
            o_ref[...] = acc_ref[...].astype(o_ref.dtype)

    return pl.pallas_call(
        body,
        name=name,
        grid=(m // tm, n // tn, nk),
        in_specs=[a_spec, b_spec],
        out_specs=pl.BlockSpec((tm, tn), lambda i, j, kk: (i, j)),
        out_shape=jax.ShapeDtypeStruct((m, n), out_dtype),
        scratch_shapes=[] if nk == 1 else [pltpu.VMEM((tm, tn), F32)],
        compiler_params=_cp("parallel", "parallel", "arbitrary"),
    )(a, b)


def _row(c):
    return pl.BlockSpec((TR, c), lambda i: (i, 0))


def _rowcol(width, cb):
    return pl.BlockSpec((TR, width), lambda i: (i, cb))


def _full(shape):
    return pl.BlockSpec(shape, lambda i: (0,) * len(shape))


def _mod_row(mc_ref, mx_ref, k, is_ctx):
    return jnp.where(is_ctx, mc_ref[k:k + 1, :], mx_ref[k:k + 1, :])


def _acc_row(ref, k, val):
    ref[k:k + 1, :] += val


def _acc_mod(ref, k, is_ctx, val):
    zero = jnp.zeros_like(val)
    ref[k:k + 1, :] += jnp.where(is_ctx, val, zero)
    ref[k + 1:k + 2, :] += jnp.where(is_ctx, zero, val)


def _prenorm(z, nw, modc, modx, i_shift, i_scale, name):
    t = z.shape[0]

    def body(z_ref, nw_ref, mc_ref, mx_ref, h_ref):
        is_ctx = pl.program_id(0) < NCT
        x = z_ref[...]
        n = x * _rstd(x) * nw_ref[...]
        h = n * (1.0 + _mod_row(mc_ref, mx_ref, i_scale, is_ctx)) + _mod_row(mc_ref, mx_ref, i_shift, is_ctx)
        h_ref[...] = h.astype(BF16)

    return pl.pallas_call(
        body, name=name, grid=(t // TR,),
        in_specs=[_row(D), _full((1, D)), _full((8, D)), _full((8, D))],
        out_specs=_row(D),
        out_shape=jax.ShapeDtypeStruct((t, D), BF16),
        compiler_params=_cp("parallel"),
    )(z, nw, modc, modx)


def _hg_lb(lb_ref, d):
    a0 = lb_ref[0, d:d + 1, :]
    a1 = lb_ref[1, d:d + 1, :]
    mx = jnp.maximum(a0, a1)
    e0 = jnp.exp(a0 - mx)
    e1 = jnp.exp(a1 - mx)
    return e0 / (e0 + e1)


def _log_sigmoid(x):
    return jnp.minimum(x, 0.0) - jnp.log(1.0 + jnp.exp(-jnp.abs(x)))


def _gates_fwd(p, hg_lb, wgk, bgk):
    t = p.shape[0]
    seg = lambda j: _rowcol(HW, MAIN0 // HW + j)

    def body(hq_ref, hi_ref, hf_ref, hb_ref, gq_ref, gk_ref, gv_ref, lr_ref, lb_ref, wgk_ref, bgk_ref,
             q_ref, v_ref, kf_ref, kb_ref, gf_ref, gb_ref):
        q_ref[:, :HW] = _silu(hq_ref[...])
        q_ref[:, HW:] = gq_ref[...] * (DH ** -0.5)
        v_ref[:, :HW] = hi_ref[...]
        v_ref[:, HW:] = gv_ref[...]
        xg = _dot(lr_ref[...].astype(BF16), wgk_ref[...], NN) + bgk_ref[...]
        for d, (raw_ref, k_ref, g_ref) in enumerate(((hf_ref, kf_ref, gf_ref), (hb_ref, kb_ref, gb_ref))):
            lbd = _hg_lb(lb_ref, d)
            f = lbd + (1.0 - lbd) * _sig(raw_ref[...])
            k_ref[:, :HW] = 1.0 - f
            k_ref[:, HW:] = gk_ref[...]
            g_ref[:, :HW] = jnp.log(f)
            g_ref[:, HW:] = _log_sigmoid(xg[:, d * HW:(d + 1) * HW]) * (1.0 / GLA_NORM)

    out = jax.ShapeDtypeStruct((t, D), F32)
    return pl.pallas_call(
        body, name="gates_fwd", grid=(t // TR,),
        in_specs=[seg(0), seg(1), seg(2), seg(3), seg(5), seg(6), seg(7), _rowcol(DH, LR0 // DH),
                  _full((2, 2, HW)), _full((DH, D)), _full((1, D))],
        out_specs=[_row(D)] * 6,
        out_shape=[out] * 6,
        compiler_params=_cp("parallel"),
    )(p, p, p, p, p, p, p, p, hg_lb, wgk, bgk)


def _post_fwd(o_fw, o_bw, p, onw):
    t = o_fw.shape[0]

    def body(of_ref, ob_ref, g1_ref, g2_ref, w_ref, y_ref):
        for h in range(NH):
            sl = slice(h * DH, (h + 1) * DH)
            o = of_ref[:, sl] + ob_ref[:, sl]
            g_ref = g1_ref if h < NH // 2 else g2_ref
            gs = slice((h % (NH // 2)) * DH, (h % (NH // 2) + 1) * DH)
            n = o * _rstd(o) * w_ref[:, sl]
            y_ref[:, sl] = (n * _silu(g_ref[:, gs])).astype(BF16)

    return pl.pallas_call(
        body, name="post_fwd", grid=(t // TR,),
        in_specs=[_row(D), _row(D), _rowcol(HW, MAIN0 // HW + 4), _rowcol(HW, MAIN0 // HW + 8), _full((1, D))],
        out_specs=_row(D),
        out_shape=jax.ShapeDtypeStruct((t, D), BF16),
        compiler_params=_cp("parallel"),
    )(o_fw, o_bw, p, p, onw)


def _merge_fwd(p, u1, u2):
    t = p.shape[0]

    def body(g1_ref, g2_ref, u1_ref, u2_ref, m_ref):
        m_ref[...] = (_sig(g1_ref[...]) * u1_ref[...] + _sig(g2_ref[...]) * u2_ref[...]).astype(BF16)

    return pl.pallas_call(
        body, name="merge_fwd", grid=(t // TR,),
        in_specs=[_rowcol(D, 0), _rowcol(D, 1), _row(D), _row(D)],
        out_specs=_row(D),
        out_shape=jax.ShapeDtypeStruct((t, D), BF16),
        compiler_params=_cp("parallel"),
    )(p, p, u1, u2)


def _mid_fwd(z, y1, nw_post, nw_pre, modc, modx):
    t = z.shape[0]

    def body(z_ref, y_ref, wpo_ref, wpr_ref, mc_ref, mx_ref, z1_ref, h_ref):
        is_ctx = pl.program_id(0) < NCT
        y = y_ref[...]
        z1 = z_ref[...] + _mod_row(mc_ref, mx_ref, 2, is_ctx) * (y * _rstd(y) * wpo_ref[...])
        z1_ref[...] = z1
        n = z1 * _rstd(z1) * wpr_ref[...]
        h = n * (1.0 + _mod_row(mc_ref, mx_ref, 4, is_ctx)) + _mod_row(mc_ref, mx_ref, 3, is_ctx)
        h_ref[...] = h.astype(BF16)

    return pl.pallas_call(
        body, name="mid_fwd", grid=(t // TR,),
        in_specs=[_row(D), _row(D), _full((1, D)), _full((1, D)), _full((8, D)), _full((8, D))],
        out_specs=[_row(D), _row(D)],
        out_shape=[jax.ShapeDtypeStruct((t, D), F32), jax.ShapeDtypeStruct((t, D), BF16)],
        compiler_params=_cp("parallel"),
    )(z, y1, nw_post, nw_pre, modc, modx)


def _swiglu_fwd(uv):
    t = uv.shape[0]

    def body(u_ref, v_ref, a_ref):
        a_ref[...] = (_silu(u_ref[...]) * v_ref[...]).astype(BF16)

    return pl.pallas_call(
        body, name="swiglu_fwd", grid=(t // TR,),
        in_specs=[_rowcol(D_FF, 0), _rowcol(D_FF, 1)],
        out_specs=_row(D_FF),
        out_shape=jax.ShapeDtypeStruct((t, D_FF), BF16),
        compiler_params=_cp("parallel"),
    )(uv, uv)


def _swiglu_bwd(uv, da):
    t = uv.shape[0]

    def body(u_ref, v_ref, da_ref, d_ref):
        u = u_ref[...]
        d = da_ref[...]
        d_ref[:, :D_FF] = (d * v_ref[...] * _dsilu(u)).astype(BF16)
        d_ref[:, D_FF:] = (d * _silu(u)).astype(BF16)

    return pl.pallas_call(
        body, name="swiglu_bwd", grid=(t // TR,),
        in_specs=[_rowcol(D_FF, 0), _rowcol(D_FF, 1), _row(D_FF)],
        out_specs=_row(2 * D_FF),
        out_shape=jax.ShapeDtypeStruct((t, 2 * D_FF), BF16),
        compiler_params=_cp("parallel"),
    )(uv, uv, da)


def _final(z1, y2, target, nw, modc, modx):
    t = z1.shape[0]

    def body(z1_ref, y_ref, tg_ref, w_ref, mc_ref, mx_ref, dz_ref, dy_ref, loss_ref, sm_ref):
        i = pl.program_id(0)
        is_ctx = i < NCT

        @pl.when(i == 0)
        def _():
            loss_ref[...] = jnp.zeros_like(loss_ref)
            sm_ref[...] = jnp.zeros_like(sm_ref)

        g = _mod_row(mc_ref, mx_ref, 5, is_ctx)
        y = y_ref[...]
        r = _rstd(y)
        w = w_ref[...]
        yr = y * r
        n = yr * w
        e = z1_ref[...] + g * n - tg_ref[...]
        lat = jnp.where(is_ctx, 0.0, 1.0)
        loss_ref[...] += lat * _colsum(e * e)
        dz = e * (lat / D)
        dz_ref[...] = dz
        _acc_mod(sm_ref, 0, is_ctx, _colsum(dz * n))
        dn = dz * g
        _acc_row(sm_ref, 2, _colsum(dn * yr))
        dy_ref[...] = _rms_bwd(dn * w, y, r).astype(BF16)

    return pl.pallas_call(
        body, name="final", grid=(t // TR,),
        in_specs=[_row(D), _row(D), pl.BlockSpec((TR, D), lambda i: (jnp.maximum(i - NCT, 0), 0)),
                  _full((1, D)), _full((8, D)), _full((8, D))],
        out_specs=[_row(D), _row(D), _full((1, D)), _full((8, D))],
        out_shape=[jax.ShapeDtypeStruct((t, D), F32), jax.ShapeDtypeStruct((t, D), BF16),
                   jax.ShapeDtypeStruct((1, D), F32), jax.ShapeDtypeStruct((8, D), F32)],
        compiler_params=_cp("arbitrary"),
    )(z1, y2, target, nw, modc, modx)


def _mid_bwd(dh2, dz, z, z1, y1, nw_post, nw_pre, modc, modx):
    t = z.shape[0]

    def body(dh_ref, dz_ref, z_ref, z1_ref, y_ref, wpo_ref, wpr_ref, mc_ref, mx_ref, dzo_ref, dy_ref, sm_ref):
        i = pl.program_id(0)
        is_ctx = i < NCT

        @pl.when(i == 0)
        def _():
            sm_ref[...] = jnp.zeros_like(sm_ref)

        dh = dh_ref[...]
        z1 = z1_ref[...]
        r = _rstd(z1)
        zr = z1 * r
        wpr = wpr_ref[...]
        n = zr * wpr
        _acc_mod(sm_ref, 0, is_ctx, _colsum(dh))
        _acc_mod(sm_ref, 2, is_ctx, _colsum(dh * n))
        dn = dh * (1.0 + _mod_row(mc_ref, mx_ref, 4, is_ctx))
        _acc_row(sm_ref, 6, _colsum(dn * zr))
        dz1 = dz_ref[...] + _rms_bwd(dn * wpr, z1, r)
        dzo_ref[...] = dz1
        y = y_ref[...]
        r1 = _rstd(y)
        yr = y * r1
        wpo = wpo_ref[...]
        g = _mod_row(mc_ref, mx_ref, 2, is_ctx)
        _acc_mod(sm_ref, 4, is_ctx, _colsum(dz1 * (yr * wpo)))
        dn1 = dz1 * g
        _acc_row(sm_ref, 7, _colsum(dn1 * yr))
        dy_ref[...] = _rms_bwd(dn1 * wpo, y, r1).astype(BF16)

    return pl.pallas_call(
        body, name="mid_bwd", grid=(t // TR,),
        in_specs=[_row(D)] * 5 + [_full((1, D)), _full((1, D)), _full((8, D)), _full((8, D))],
        out_specs=[_row(D), _row(D), _full((8, D))],
        out_shape=[jax.ShapeDtypeStruct((t, D), F32), jax.ShapeDtypeStruct((t, D), BF16),
                   jax.ShapeDtypeStruct((8, D), F32)],
        compiler_params=_cp("arbitrary"),
    )(dh2, dz, z, z1, y1, nw_post, nw_pre, modc, modx)


def _pre_bwd(dh1, dz, z, nw, modc, modx):
    t = z.shape[0]

    def body(dh_ref, dz_ref, z_ref, w_ref, mc_ref, mx_ref, dzo_ref, sm_ref):
        i = pl.program_id(0)
        is_ctx = i < NCT

        @pl.when(i == 0)
        def _():
            sm_ref[...] = jnp.zeros_like(sm_ref)

        dh = dh_ref[...]
        x = z_ref[...]
        r = _rstd(x)
        xr = x * r
        w = w_ref[...]
        _acc_mod(sm_ref, 0, is_ctx, _colsum(dh))
        _acc_mod(sm_ref, 2, is_ctx, _colsum(dh * (xr * w)))
        dn = dh * (1.0 + _mod_row(mc_ref, mx_ref, 1, is_ctx))
        _acc_row(sm_ref, 4, _colsum(dn * xr))
        dzo_ref[...] = dz_ref[...] + _rms_bwd(dn * w, x, r)

    return pl.pallas_call(
        body, name="pre_bwd", grid=(t // TR,),
        in_specs=[_row(D)] * 3 + [_full((1, D)), _full((8, D)), _full((8, D))],
        out_specs=[_row(D), _full((8, D))],
        out_shape=[jax.ShapeDtypeStruct((t, D), F32), jax.ShapeDtypeStruct((8, D), F32)],
        compiler_params=_cp("arbitrary"),
    )(dh1, dz, z, nw, modc, modx)


def _merge_bwd(dm, p, u1, u2):
    t = dm.shape[0]

    def body(dm_ref, g1_ref, g2_ref, u1_ref, u2_ref, du1_ref, du2_ref, dg_ref):
        dm_ = dm_ref[...]
        s1 = _sig(g1_ref[...])
        s2 = _sig(g2_ref[...])
        du1_ref[...] = (dm_ * s1).astype(BF16)
        du2_ref[...] = (dm_ * s2).astype(BF16)
        dg_ref[:, :D] = (dm_ * u1_ref[...] * s1 * (1.0 - s1)).astype(BF16)
        dg_ref[:, D:] = (dm_ * u2_ref[...] * s2 * (1.0 - s2)).astype(BF16)

    return pl.pallas_call(
        body, name="merge_bwd", grid=(t // TR,),
        in_specs=[_row(D), _rowcol(D, 0), _rowcol(D, 1), _row(D), _row(D)],
        out_specs=[_row(D), _row(D), _row(2 * D)],
        out_shape=[jax.ShapeDtypeStruct((t, D), BF16), jax.ShapeDtypeStruct((t, D), BF16),
                   jax.ShapeDtypeStruct((t, 2 * D), BF16)],
        compiler_params=_cp("parallel"),
    )(dm, p, p, u1, u2)


def _post_bwd(dy_hg, dy_gla, o_fw, o_bw, p, onw):
    t = o_fw.shape[0]

    def body(d1_ref, d2_ref, of_ref, ob_ref, g1_ref, g2_ref, w_ref, do_ref, dg_ref, sm_ref):
        @pl.when(pl.program_id(0) == 0)
        def _():
            sm_ref[...] = jnp.zeros_like(sm_ref)

        for h in range(NH):
            sl = slice(h * DH, (h + 1) * DH)
            gs = slice((h % (NH // 2)) * DH, (h % (NH // 2) + 1) * DH)
            g_ref, d_ref = (g1_ref, d1_ref) if h < NH // 2 else (g2_ref, d2_ref)
            o = of_ref[:, sl] + ob_ref[:, sl]
            r = _rstd(o)
            orr = o * r
            w = w_ref[:, sl]
            gt = g_ref[:, gs]
            dy = d_ref[:, gs]
            dg_ref[:, sl] = (dy * (orr * w) * _dsilu(gt)).astype(BF16)
            dn = dy * _silu(gt)
            sm_ref[0:1, sl] += _colsum(dn * orr)
            do_ref[:, sl] = _rms_bwd(dn * w, o, r)

    return pl.pallas_call(
        body, name="post_bwd", grid=(t // TR,),
        in_specs=[_row(HW), _row(HW), _row(D), _row(D), _rowcol(HW, MAIN0 // HW + 4), _rowcol(HW, MAIN0 // HW + 8),
                  _full((1, D))],
        out_specs=[_row(D), _row(D), _full((8, D))],
        out_shape=[jax.ShapeDtypeStruct((t, D), F32), jax.ShapeDtypeStruct((t, D), BF16),
                   jax.ShapeDtypeStruct((8, D), F32)],
        compiler_params=_cp("arbitrary"),
    )(dy_hg, dy_gla, o_fw, o_bw, p, p, onw)


def _gates_bwd(p, hg_lb, wgk, bgk, dgm, dgo, dq_f, dq_b, dv_f, dv_b, dk_f, dk_b, dg_f, dg_b):
    t = p.shape[0]
    seg = lambda j: _rowcol(HW, MAIN0 // HW + j)

    def body(hq_ref, hf_ref, hb_ref, lr_ref, lb_ref, wgk_ref, bgk_ref, dgm_ref, dgo_ref,
             dqf_ref, dqb_ref, dvf_ref, dvb_ref, dkf_ref, dkb_ref, dgf_ref, dgb_ref,
             dp_ref, dlb_ref, dw_ref, db_ref):
        @pl.when(pl.program_id(0) == 0)
        def _():
            dlb_ref[...] = jnp.zeros_like(dlb_ref)
            dw_ref[...] = jnp.zeros_like(dw_ref)
            db_ref[...] = jnp.zeros_like(db_ref)

        c0 = MAIN0

        def put(j, val):
            dp_ref[:, c0 + j * HW:c0 + (j + 1) * HW] = val.astype(BF16)

        dp_ref[:, :MAIN0] = dgm_ref[...]
        dq = dqf_ref[...] + dqb_ref[...]
        dv = dvf_ref[...] + dvb_ref[...]
        put(0, dq[:, :HW] * _dsilu(hq_ref[...]))
        put(1, dv[:, :HW])
        put(5, dq[:, HW:] * (DH ** -0.5))
        put(7, dv[:, HW:])
        put(6, dkf_ref[:, HW:] + dkb_ref[:, HW:])
        dp_ref[:, c0 + 4 * HW:c0 + 5 * HW] = dgo_ref[:, :HW]
        dp_ref[:, c0 + 8 * HW:c0 + 9 * HW] = dgo_ref[:, HW:]
        lr = lr_ref[...].astype(BF16)
        xg = _dot(lr, wgk_ref[...], NN) + bgk_ref[...]
        dxg = []
        for d, (raw_ref, dk_ref, dg_ref) in enumerate(((hf_ref, dkf_ref, dgf_ref), (hb_ref, dkb_ref, dgb_ref))):
            lbd = _hg_lb(lb_ref, d)
            s = _sig(raw_ref[...])
            f = lbd + (1.0 - lbd) * s
            df = dg_ref[:, :HW] / f - dk_ref[:, :HW]
            put(2 + d, df * (1.0 - lbd) * s * (1.0 - s))
            dlb_ref[d:d + 1, :] += _colsum(df * (1.0 - s)) * (lbd * (1.0 - lbd))
            dxg.append(dg_ref[:, HW:] * (1.0 / GLA_NORM) * _sig(-xg[:, d * HW:(d + 1) * HW]))
        dxg = jnp.concatenate(dxg, axis=1)
        db_ref[0:1, :] += _colsum(dxg)
        dxg_b = dxg.astype(BF16)
        dw_ref[...] += _dot(lr, dxg_b, TN)
        dp_ref[:, LR0:LR0 + DH] = _dot(dxg_b, wgk_ref[...], NT).astype(BF16)
        dp_ref[:, LR0 + DH:] = jnp.zeros((TR, W_IN_COLS - LR0 - DH), BF16)

    return pl.pallas_call(
        body, name="gates_bwd", grid=(t // TR,),
        in_specs=[seg(0), seg(2), seg(3), _rowcol(DH, LR0 // DH), _full((2, 2, HW)), _full((DH, D)), _full((1, D)),
                  _row(2 * D), _row(D)] + [_row(D)] * 8,
        out_specs=[_row(W_IN_COLS), _full((8, HW)), _full((DH, D)), _full((8, D))],
        out_shape=[jax.ShapeDtypeStruct((t, W_IN_COLS), BF16), jax.ShapeDtypeStruct((8, HW), F32),
                   jax.ShapeDtypeStruct((DH, D), F32), jax.ShapeDtypeStruct((8, D), F32)],
        compiler_params=_cp("arbitrary"),
    )(p, p, p, p, hg_lb, wgk, bgk, dgm, dgo, dq_f, dq_b, dv_f, dv_b, dk_f, dk_b, dg_f, dg_b)


def _scan_consts(rev):
    r = lax.broadcasted_iota(jnp.int32, (CHUNK, CHUNK), 0)
    u = lax.broadcasted_iota(jnp.int32, (CHUNK, CHUNK), 1)
    rp = lax.broadcasted_iota(jnp.int32, (CHUNK, 1), 0)
    if rev:
        r, u, rp = CHUNK - 1 - r, CHUNK - 1 - u, CHUNK - 1 - rp
    tri = jnp.where(u <= r, 1.0, 0.0).astype(F32)
    tri_t = jnp.where(r <= u, 1.0, 0.0).astype(F32)
    lv = []
    for b in LEVELS:
        sh = b.bit_length() - 1
        pair = ((r >> sh) == (u >> sh) + 1) & (((u >> sh) & 1) == 0)
        pair_t = ((u >> sh) == (r >> sh) + 1) & (((r >> sh) & 1) == 0)
        tside = ((rp >> sh) & 1) == 1
        lv.append((pair, pair_t, tside))
    bd = LEVELS[-1].bit_length() - 1
    diag = ((r >> bd) == (u >> bd)) & (u <= r)
    diag_t = ((r >> bd) == (u >> bd)) & (r <= u)
    return tri, tri_t, lv, diag, diag_t


def _row_of(pos, rev):
    return CHUNK - 1 - pos if rev else pos


def _chunk_terms(q, k, g, b_scr, consts, rev):
    tri, _, lv, _, _ = consts
    cum = _dot(tri, g, NN, precision=HI)
    b_scr[...] = cum
    terms = []
    for b, (_, _, tside) in zip(LEVELS, lv):
        pieces = []
        for j in range(CHUNK // (2 * b)):
            row = _row_of(2 * b * j + b - 1, rev)
            pieces.append(jnp.broadcast_to(b_scr[row:row + 1, :], (2 * b, DH)))
        if rev:
            pieces = pieces[::-1]
        bnd = pieces[0] if len(pieces) == 1 else jnp.concatenate(pieces, axis=0)
        w = jnp.exp(jnp.minimum(jnp.where(tside, cum - bnd, bnd - cum), 0.0))
        wq = jnp.where(tside, w, 0.0)
        wk = jnp.where(tside, 0.0, w)
        terms.append((wq, wk))
    b = LEVELS[-1]
    pieces = []
    for j in range(CHUNK // b):
        if j == 0:
            pieces.append(jnp.zeros((b, DH), F32))
        else:
            row = _row_of(b * j - 1, rev)
            pieces.append(jnp.broadcast_to(b_scr[row:row + 1, :], (b, DH)))
    if rev:
        pieces = pieces[::-1]
    start = jnp.concatenate(pieces, axis=0)
    wq = jnp.exp(jnp.minimum(cum - start, 0.0))
    wk = jnp.exp(jnp.minimum(start - cum, EXP_CLAMP))
    terms.append((wq, wk))
    return cum, terms


def _chunk_order(i, nc, rev):
    nctx = CTX // CHUNK
    if not rev:
        return i
    return jnp.where(i < nctx, nctx - 1 - i, nc - 1 - (i - nctx))


def _scan_fwd(q, k, v, g, rev):
    BF16, _dot = SCAN_FWD_DT, functools.partial(_prec_dot, SCAN_FWD_DT)
    t = q.shape[0]
    nc = t // CHUNK

    def body(q_ref, k_ref, v_ref, g_ref, o_ref, st_ref, s_scr, b_scr):
        consts = _scan_consts(rev)
        _, _, lv, diag, _ = consts
        s_scr[...] = jnp.zeros_like(s_scr)

        def step(i, carry):
            c = _chunk_order(i, nc, rev)
            rows = pl.ds(pl.multiple_of(c * CHUNK, CHUNK), CHUNK)
            qc, kc, vc, gc = q_ref[rows, :], k_ref[rows, :], v_ref[rows, :], g_ref[rows, :]
            cum, terms = _chunk_terms(qc, kc, gc, b_scr, consts, rev)
            masks = [pair for pair, _, _ in lv] + [diag]
            a = jnp.zeros((CHUNK, CHUNK), F32)
            for (wq, wk), m in zip(terms, masks):
                sc = _dot((qc * wq).astype(BF16), (kc * wk).astype(BF16), NT)
                a = a + jnp.where(m, sc, 0.0)
            st = s_scr[...]
            st_ref[0, c] = st
            tot = _colsum(gc)
            qe = (qc * jnp.exp(cum)).astype(BF16)
            vb = vc.astype(BF16)
            o_ref[rows, :] = _dot(a.astype(BF16), vb, NN) + _dot(qe, st.astype(BF16), NT)
            ke = (kc * jnp.exp(tot - cum)).astype(BF16)
            s_scr[...] = st * jnp.exp(tot) + _dot(vb, ke, TN)
            return carry

        lax.fori_loop(0, nc, step, 0)

    col = pl.BlockSpec((t, DH), lambda h: (0, h))
    return pl.pallas_call(
        body, name="scan_fwd_" + ("bw" if rev else "fw"), grid=(NH,),
        in_specs=[col] * 4,
        out_specs=[col, pl.BlockSpec((1, nc, DH, DH), lambda h: (h, 0, 0, 0))],
        out_shape=[jax.ShapeDtypeStruct((t, D), F32), jax.ShapeDtypeStruct((NH, nc, DH, DH), F32)],
        scratch_shapes=[pltpu.VMEM((DH, DH), F32), pltpu.VMEM((CHUNK, DH), F32)],
        compiler_params=_cp("parallel"),
    )(q, k, v, g)


def _scan_bwd(q, k, v, g, do, states, rev):
    BF16, _dot = SCAN_BWD_DT, functools.partial(_prec_dot, SCAN_BWD_DT)
    t = q.shape[0]
    nc = t // CHUNK

    def body(q_ref, k_ref, v_ref, g_ref, do_ref, st_ref, dq_ref, dk_ref, dv_ref, dg_ref, ds_scr, b_scr):
        consts = _scan_consts(rev)
        _, tri_t, lv, diag, diag_t = consts
        ds_scr[...] = jnp.zeros_like(ds_scr)

        def step(ii, carry):
            c = _chunk_order(nc - 1 - ii, nc, rev)
            rows = pl.ds(pl.multiple_of(c * CHUNK, CHUNK), CHUNK)
            qc, kc, vc, gc = q_ref[rows, :], k_ref[rows, :], v_ref[rows, :], g_ref[rows, :]
            dob = do_ref[rows, :].astype(BF16)
            vb = vc.astype(BF16)
            cum, terms = _chunk_terms(qc, kc, gc, b_scr, consts, rev)
            masks = [(pair, pair_t) for pair, pair_t, _ in lv] + [(diag, diag_t)]
            da = _dot(dob, vb, NT)
            da_t = _dot(vb, dob, NT)
            a_t = jnp.zeros((CHUNK, CHUNK), F32)
            dq = jnp.zeros((CHUNK, DH), F32)
            dk = jnp.zeros((CHUNK, DH), F32)
            db = jnp.zeros((CHUNK, DH), F32)
            for (wq, wk), (m, m_t) in zip(terms, masks):
                qt = qc * wq
                kt = kc * wk
                qtb = qt.astype(BF16)
                ktb = kt.astype(BF16)
                a_t = a_t + jnp.where(m_t, _dot(ktb, qtb, NT), 0.0)
                dqt = _dot(jnp.where(m, da, 0.0).astype(BF16), ktb, NN)
                dkt = _dot(jnp.where(m_t, da_t, 0.0).astype(BF16), qtb, NN)
                dq = dq + dqt * wq
                dk = dk + dkt * wk
                db = db + dqt * qt - dkt * kt
            st = st_ref[0, c]
            dst = ds_scr[...]
            tot = _colsum(gc)
            e_tot = jnp.exp(tot)
            e_b = jnp.exp(cum)
            e_t = jnp.exp(tot - cum)
            qe = qc * e_b
            ke = kc * e_t
            dstb = dst.astype(BF16)
            dv_ref[rows, :] = _dot(a_t.astype(BF16), dob, NN) + _dot(ke.astype(BF16), dstb, NT)
            dqe = _dot(dob, st.astype(BF16), NN)
            dke = _dot(vb, dstb, NN)
            dq_ref[rows, :] = dq + dqe * e_b
            dk_ref[rows, :] = dk + dke * e_t
            db = db + dqe * qe - dke * ke
            dtot = _colsum(dst * st) * e_tot + _colsum(dke * ke)
            dg_ref[rows, :] = _dot(tri_t, db, NN, precision=HI) + dtot
            ds_scr[...] = dst * e_tot + _dot(dob, qe.astype(BF16), TN)
            return carry

        lax.fori_loop(0, nc, step, 0)

    col = pl.BlockSpec((t, DH), lambda h: (0, h))
    out = jax.ShapeDtypeStruct((t, D), F32)
    return pl.pallas_call(
        body, name="scan_bwd_" + ("bw" if rev else "fw"), grid=(NH,),
        in_specs=[col] * 5 + [pl.BlockSpec((1, nc, DH, DH), lambda h: (h, 0, 0, 0))],
        out_specs=[col] * 4,
        out_shape=[out] * 4,
        scratch_shapes=[pltpu.VMEM((DH, DH), F32), pltpu.VMEM((CHUNK, DH), F32)],
        compiler_params=_cp("parallel"),
    )(q, k, v, g, do, states)


W_IN_REF = 6688
GATE0 = 4640
LRW = 32


def _layout_w_in(w):
    return jnp.concatenate([w[:, GATE0:], w[:, :GATE0 - LRW], w[:, GATE0 - LRW:GATE0],
                            jnp.zeros((w.shape[0], W_IN_COLS - W_IN_REF), w.dtype)], axis=1)


def _unlayout_w_in(d):
    return jnp.concatenate([d[:, MAIN0:LR0 + LRW], d[:, :MAIN0]], axis=1)


def _layout_wgk(w):
    r = w.shape[1]
    top = jnp.concatenate([w[0], jnp.zeros_like(w[0])], axis=1)
    bot = jnp.concatenate([jnp.zeros_like(w[1]), w[1]], axis=1)
    return jnp.concatenate([top, bot, jnp.zeros((DH - 2 * r, D), w.dtype)], axis=0)


def _unlayout_wgk(d, r=16):
    return jnp.stack([d[:r, :HW], d[r:2 * r, HW:]])


def _local_step(z, target, modc, modx, norms, onw, hg_lb, wgk, bgk, w_in, w_br_hg, w_br_gla, w_out, w_gu, w_down):
    n_pre1, n_post1, n_pre2, n_post2 = norms
    t = z.shape[0]
    tm = 768 if t % 768 == 0 else 256
    h1 = _prenorm(z, n_pre1, modc, modx, 0, 1, "prenorm1")
    p = _matmul(h1, w_in, NN, F32, "mm_in", tm, 512, D)
    q, v, k_f, k_b, g_f, g_b = _gates_fwd(p, hg_lb, wgk, bgk)
    o_f, st_f = _scan_fwd(q, k_f, v, g_f, False)
    o_b, st_b = _scan_fwd(q, k_b, v, g_b, True)
    y = _post_fwd(o_f, o_b, p, onw)
    u1 = _matmul(y, w_br_hg, NN, F32, "mm_br_hg", tm, 512, HW, a_off=0)
    u2 = _matmul(y, w_br_gla, NN, F32, "mm_br_gla", tm, 512, HW, a_off=1)
    merged = _merge_fwd(p, u1, u2)
    y1 = _matmul(merged, w_out, NN, F32, "mm_out", tm, 512, D)
    z1, h2 = _mid_fwd(z, y1, n_post1, n_pre2, modc, modx)
    uv = _matmul(h2, w_gu, NN, F32, "mm_gu", tm, 512, D)
    act = _swiglu_fwd(uv)
    y2 = _matmul(act, w_down, NN, F32, "mm_down", tm, 512, D_FF // 2)
    dz, dy2, loss_vec, sm_final = _final(z1, y2, target, n_post2, modc, modx)
    dact = _matmul(dy2, w_down, NT, F32, "mm_down_dx", tm, D_FF // 2, D)
    d_w_down = _matmul(act, dy2, TN, F32, "mm_down_dw", D_FF // 2, 512, t)
    duv = _swiglu_bwd(uv, dact)
    dh2 = _matmul(duv, w_gu, NT, F32, "mm_gu_dx", tm, 512, D_FF // 2)
    d_w_gu = _matmul(h2, duv, TN, F32, "mm_gu_dw", 512, 512, t)
    dz, dy1, sm_mid = _mid_bwd(dh2, dz, z, z1, y1, n_post1, n_pre2, modc, modx)
    dmerged = _matmul(dy1, w_out, NT, F32, "mm_out_dx", tm, 512, D)
    d_w_out = _matmul(merged, dy1, TN, F32, "mm_out_dw", 512, 512, t)
    du1, du2, dgm = _merge_bwd(dmerged, p, u1, u2)
    dy_hg = _matmul(du1, w_br_hg, NT, F32, "mm_br_hg_dx", tm, HW, D)
    dy_gla = _matmul(du2, w_br_gla, NT, F32, "mm_br_gla_dx", tm, HW, D)
    d_w_br_hg = _matmul(y, du1, TN, F32, "mm_br_hg_dw", HW, 512, t, a_off=0, m_out=HW)
    d_w_br_gla = _matmul(y, du2, TN, F32, "mm_br_gla_dw", HW, 512, t, a_off=1, m_out=HW)
    do, dgo, sm_post = _post_bwd(dy_hg, dy_gla, o_f, o_b, p, onw)
    dq_f, dk_f, dv_f, dg_f = _scan_bwd(q, k_f, v, g_f, do, st_f, False)
    dq_b, dk_b, dv_b, dg_b = _scan_bwd(q, k_b, v, g_b, do, st_b, True)
    dp, d_lb, d_wgk, d_bgk = _gates_bwd(p, hg_lb, wgk, bgk, dgm, dgo, dq_f, dq_b, dv_f, dv_b, dk_f, dk_b, dg_f, dg_b)
    dh1 = _matmul(dp, w_in, NT, F32, "mm_in_dx", tm, 512, 1024)
    d_w_in = _matmul(h1, dp, TN, F32, "mm_in_dw", 512, 512, t)
    dz, sm_pre = _pre_bwd(dh1, dz, z, n_pre1, modc, modx)
    return dict(loss_vec=loss_vec, dz=dz, sm_final=sm_final, sm_mid=sm_mid, sm_post=sm_post, sm_pre=sm_pre,
                d_lb=d_lb, d_wgk=d_wgk, d_bgk=d_bgk, d_w_in=d_w_in, d_w_br_hg=d_w_br_hg, d_w_br_gla=d_w_br_gla,
                d_w_out=d_w_out, d_w_gu=d_w_gu, d_w_down=d_w_down)


MESH = pl.DeviceIdType.MESH
ANY = pl.BlockSpec(memory_space=pl.ANY)
N_REL = N_DEV - 1


def _place():
    return lax.axis_index("x"), lax.axis_index("y"), lax.axis_index("c")


def _slot(p):
    return 4 * p[0] + 2 * p[1] + p[2]


def _all_gather(arrays, name):
    n = len(arrays)

    def body(*refs):
        ins, outs = refs[:n], refs[n:2 * n]
        send_sems, recv_sems, local_sems = refs[2 * n:]
        x, y, c = _place()
        me, sibling = (x, y, c), (x, y, 1 - c)
        chips = [(1 - x, y), (x, 1 - y), (1 - x, 1 - y)]

        def copy(a, k, block, to, src=None):
            dst = outs[a].at[_slot(block)]
            return pltpu.make_async_remote_copy(
                src_ref=dst if src is None else src, dst_ref=dst,
                send_sem=send_sems.at[N_REL * a + k], recv_sem=recv_sems.at[N_REL * a + k],
                device_id=to, device_id_type=MESH)

        mine = [pltpu.make_async_copy(ins[a], outs[a].at[_slot(me)], local_sems.at[a]) for a in range(n)]
        for cp in mine:
            cp.start()
        first = []
        for a in range(n):
            first.append(copy(a, 0, me, sibling, src=ins[a]))
            first += [copy(a, 1 + j, me, (*chip, c), src=ins[a]) for j, chip in enumerate(chips)]
        for cp in first:
            cp.start()
        passed = []
        for j, chip in enumerate(chips):
            for a in range(n):
                copy(a, 1 + j, (*chip, c), me).wait_recv()
                fwd = copy(a, 4 + j, (*chip, c), sibling)
                fwd.start()
                passed.append(fwd)
        for a in range(n):
            copy(a, 0, sibling, me).wait_recv()
        for j, chip in enumerate(chips):
            for a in range(n):
                copy(a, 4 + j, (*chip, 1 - c), me).wait_recv()
        for cp in first + passed:
            cp.wait_send()
        for cp in mine:
            cp.wait()

    return pl.pallas_call(
        body, name=name,
        in_specs=[ANY] * n, out_specs=[ANY] * n,
        out_shape=[jax.ShapeDtypeStruct((N_DEV,) + a.shape, a.dtype) for a in arrays],
        scratch_shapes=[pltpu.SemaphoreType.DMA((N_REL * n,)), pltpu.SemaphoreType.DMA((N_REL * n,)),
                        pltpu.SemaphoreType.DMA((n,))],
    )(*arrays)


def _exchange(arrays, name):
    n = len(arrays)

    def body(*refs):
        ins, outs = refs[:n], refs[n:2 * n]
        send_sems, recv_sems, local_sems = refs[2 * n:]
        x, y, c = _place()
        me = _slot((x, y, c))
        mine = [pltpu.make_async_copy(ins[a].at[me], outs[a].at[me], local_sems.at[a]) for a in range(n)]
        for cp in mine:
            cp.start()
        copies = []
        for a in range(n):
            for k in range(1, N_DEV):
                flip = lambda v, bit: 1 - v if bit else v
                peer = (flip(x, k & 4), flip(y, k & 2), flip(c, k & 1))
                copies.append(pltpu.make_async_remote_copy(
                    src_ref=ins[a].at[_slot(peer)], dst_ref=outs[a].at[me],
                    send_sem=send_sems.at[N_REL * a + k - 1], recv_sem=recv_sems.at[N_REL * a + k - 1],
                    device_id=peer, device_id_type=MESH))
                copies[-1].start()
        i = 0
        for a in range(n):
            for k in range(1, N_DEV):
                flip = lambda v, bit: 1 - v if bit else v
                peer = (flip(x, k & 4), flip(y, k & 2), flip(c, k & 1))
                pltpu.make_async_remote_copy(
                    src_ref=ins[a].at[_slot(peer)], dst_ref=outs[a].at[_slot(peer)],
                    send_sem=send_sems.at[N_REL * a + k - 1], recv_sem=recv_sems.at[N_REL * a + k - 1],
                    device_id=peer, device_id_type=MESH).wait_recv()
                i += 1
        for cp in copies:
            cp.wait_send()
        for cp in mine:
            cp.wait()

    return pl.pallas_call(
        body, name=name,
        in_specs=[ANY] * n, out_specs=[ANY] * n,
        out_shape=[jax.ShapeDtypeStruct(a.shape, a.dtype) for a in arrays],
        scratch_shapes=[pltpu.SemaphoreType.DMA((N_REL * n,)), pltpu.SemaphoreType.DMA((N_REL * n,)),
                        pltpu.SemaphoreType.DMA((n,))],
    )(*arrays)


def _mod_fwd(a, w, b):
    def body(a_ref, w_ref, b_ref, o_ref):
        o_ref[...] = _dot(_silu(a_ref[...]), w_ref[...], NN, precision=HI) + b_ref[...]

    return pl.pallas_call(
        body, name="mod_fwd", out_shape=jax.ShapeDtypeStruct((a.shape[0], w.shape[1]), F32),
        compiler_params=pltpu.CompilerParams(vmem_limit_bytes=VMEM_LIMIT),
    )(a, w, b)


def _mod_bwd(a, d, w):
    def body(a_ref, d_ref, w_ref, dw_ref, dc_ref):
        av = a_ref[...]
        dv = d_ref[...]
        dw_ref[...] = _dot(_silu(av), dv, TN, precision=HI)
        da = _dot(dv[0:8, :], w_ref[...], NT, precision=HI) * _dsilu(av[0:8, :])
        row = lax.broadcasted_iota(jnp.int32, da.shape, 0)
        dc_ref[...] = jnp.where(row == 0, da, 0.0)

    return pl.pallas_call(
        body, name="mod_bwd",
        out_shape=[jax.ShapeDtypeStruct(w.shape, F32), jax.ShapeDtypeStruct((8, w.shape[0]), F32)],
        compiler_params=pltpu.CompilerParams(vmem_limit_bytes=VMEM_LIMIT),
    )(a, d, w)


def _sum_devices(g):
    def body(g_ref, o_ref):
        acc = g_ref[0]
        for i in range(1, g.shape[0]):
            acc = acc + g_ref[i]
        o_ref[...] = acc

    return pl.pallas_call(body, name="sum_devices_%d" % g.shape[1],
                          out_shape=jax.ShapeDtypeStruct(g.shape[1:], F32))(g)


def _adam_rows(r, c, n):
    budget = 6 * 1024 * 1024
    best = None
    for tr in range(16, r + 1, 16):
        if r % tr == 0 and tr * c * (2 * n + 28) <= budget:
            best = tr
    return best if best is not None else r


def _adamw(g, w, m, v, name):
    n, r, c = g.shape
    tr = _adam_rows(r, c, n)
    bc1 = 1.0 - ADAM_B1 ** ADAM_STEP
    bc2 = 1.0 - ADAM_B2 ** ADAM_STEP

    def body(g_ref, w_ref, m_ref, v_ref, go_ref, d_ref, mo_ref, vo_ref):
        grad = g_ref[0].astype(F32)
        for i in range(1, n):
            grad = grad + g_ref[i].astype(F32)
        go_ref[...] = grad
        m_new = ADAM_B1 * m_ref[...] + (1.0 - ADAM_B1) * grad
        v_new = ADAM_B2 * v_ref[...] + (1.0 - ADAM_B2) * (grad * grad)
        mo_ref[...] = m_new
        vo_ref[...] = v_new
        d_ref[...] = -ADAM_LR * ((m_new / bc1) / (jnp.sqrt(v_new / bc2) + ADAM_EPS) + ADAM_WD * w_ref[...])

    blk = pl.BlockSpec((tr, c), lambda i: (i, 0))
    out = jax.ShapeDtypeStruct((r, c), F32)
    return pl.pallas_call(
        body, name=name, grid=(r // tr,),
        in_specs=[pl.BlockSpec((n, tr, c), lambda i: (0, i, 0)), blk, blk, blk],
        out_specs=[blk] * 4, out_shape=[out] * 4,
        compiler_params=_cp("parallel"),
    )(g, w, m, v)


def kernel(x, c, ctx, c_ctx, w_mod, b_mod, norm_pre1, norm_post1, norm_pre2, norm_post2, w_in, hg_lb, hg_onorm, gla_w_gk, gla_b_gk, gla_onorm, w_br_hg, w_br_gla, w_out, w_ff_gate, w_ff_up, w_ff_down, loss_target, m_c_ctx, m_w_mod, m_b_mod, m_norm_pre1, m_norm_post1, m_norm_pre2, m_norm_post2, m_w_in, m_hg_lb, m_hg_onorm, m_gla_w_gk, m_gla_b_gk, m_gla_onorm, m_w_br_hg, m_w_br_gla, m_w_out, m_w_ff_gate, m_w_ff_up, m_w_ff_down, v_c_ctx, v_w_mod, v_b_mod, v_norm_pre1, v_norm_post1, v_norm_pre2, v_norm_post2, v_w_in, v_hg_lb, v_hg_onorm, v_gla_w_gk, v_gla_b_gk, v_gla_onorm, v_w_br_hg, v_w_br_gla, v_w_out, v_w_ff_gate, v_w_ff_up, v_w_ff_down):
    xi, yi, ci = lax.axis_index("x"), lax.axis_index("y"), lax.axis_index("c")
    me = 4 * xi + 2 * yi + ci
    t = CTX + x.shape[1]

    c_all, lb_g, wgk_g, bgk_g = _all_gather([c, hg_lb, gla_w_gk[0], gla_b_gk[0]], "ag_small")
    big = [w_in[0], w_br_hg[0], w_br_gla[0], w_out[0], w_ff_gate[0], w_ff_up[0], w_ff_down[0]]
    g_in, g_brh, g_brg, g_out, g_gate, g_up, g_down = _all_gather([w.astype(BF16) for w in big], "ag_weights")
    cols = lambda g: jnp.transpose(g, (1, 0, 2)).reshape(g.shape[1], N_DEV * g.shape[2])
    w_in_k = _layout_w_in(cols(g_in))
    w_gu_k = jnp.concatenate([cols(g_gate), cols(g_up)], axis=1)
    w_down_k = g_down.reshape(D_FF, D)
    w_out_k = g_out.reshape(D, D)
    hg_lb_full = jnp.transpose(lb_g, (1, 2, 0, 3)).reshape(2, 2, HW)
    wgk_k = _layout_wgk(jnp.transpose(wgk_g, (1, 2, 0, 3)).reshape(2, 16, HW)).astype(BF16)
    bgk_k = jnp.transpose(bgk_g, (1, 0, 2)).reshape(1, D)
    onw = jnp.concatenate([jnp.tile(hg_onorm, (1, NH // 2)), jnp.tile(gla_onorm, (1, NH // 2))], axis=1)

    n_mod = w_mod.shape[2]
    a9 = jnp.concatenate([c_ctx[None], c_all[:, 0], jnp.zeros((16 - 1 - N_DEV, D), F32)], axis=0)
    b_loc = lax.dynamic_slice(b_mod, (0, me * n_mod), (1, n_mod))
    s_loc = _mod_fwd(a9, w_mod[0], b_loc)
    s_all, = _all_gather([s_loc], "ag_mod")
    mod_all = jnp.transpose(s_all, (1, 0, 2)).reshape(16, N_DEV * n_mod)
    pad8 = lambda m: jnp.concatenate([m.reshape(6, D), jnp.zeros((2, D), F32)], axis=0)
    modc = pad8(mod_all[0])
    modx = pad8(lax.dynamic_slice(mod_all, (1 + me, 0), (1, N_DEV * n_mod))[0])

    z = jnp.concatenate([ctx[0], x[0]], axis=0)
    norms = (norm_pre1, norm_post1, norm_pre2, norm_post2)
    r = _local_step(z, loss_target[0], modc, modx, norms, onw, hg_lb_full, wgk_k, bgk_k,
                    w_in_k, cols(g_brh), cols(g_brg), w_out_k, w_gu_k, w_down_k)
    loss = lax.psum((0.5 / D) * jnp.sum(r["loss_vec"]), ("x", "y", "c"))
    grad_x = r["dz"][CTX:][None]

    sm_pre, sm_mid, sm_fin = r["sm_pre"], r["sm_mid"], r["sm_final"]
    dmodc = jnp.stack([sm_pre[0], sm_pre[2], sm_mid[4], sm_mid[0], sm_mid[2], sm_fin[0]]).reshape(-1)
    dmodx = jnp.stack([sm_pre[1], sm_pre[3], sm_mid[5], sm_mid[1], sm_mid[3], sm_fin[1]]).reshape(-1)
    on = r["sm_post"][0].reshape(NH, DH)
    pieces = [dmodc, dmodx, sm_pre[4], sm_mid[7], sm_mid[6], sm_fin[2], on[:NH // 2].sum(0), on[NH // 2:].sum(0),
              r["d_lb"][:2].reshape(-1), _unlayout_wgk(r["d_wgk"]).reshape(-1), r["d_bgk"][0]]
    sizes = [p.shape[0] for p in pieces]
    pack = jnp.concatenate(pieces).reshape(-1, DH)
    pack_all, = _all_gather([pack], "ag_small_grads")
    tot = _sum_devices(pack_all).reshape(-1)
    offs = [sum(sizes[:i]) for i in range(len(sizes))]
    part = lambda i: tot[offs[i]:offs[i] + sizes[i]]
    dmodc_t, dmodx_t = part(0), part(1)
    g_b_mod = (dmodc_t + dmodx_t)[None]
    g_norms = [part(i)[None] for i in (2, 3, 4, 5)]
    g_hg_on, g_gla_on = part(6)[None], part(7)[None]
    lb0 = lax.dynamic_slice(part(8).reshape(2, HW), (0, me * (HW // N_DEV)), (2, HW // N_DEV))
    g_hg_lb = jnp.stack([lb0, -lb0])
    g_wgk = lax.dynamic_slice(part(9).reshape(2, 16, HW), (0, 0, me * (HW // N_DEV)), (2, 16, HW // N_DEV))[None]
    g_bgk = lax.dynamic_slice(part(10).reshape(2, HW), (0, me * (HW // N_DEV)), (2, HW // N_DEV))[None]

    dmx_all = pack_all.reshape(N_DEV, -1)[:, sizes[0]:sizes[0] + sizes[1]]
    d9 = jnp.concatenate([lax.dynamic_slice(dmodc_t[None], (0, me * n_mod), (1, n_mod)),
                          lax.dynamic_slice(dmx_all, (0, me * n_mod), (N_DEV, n_mod)),
                          jnp.zeros((16 - 1 - N_DEV, n_mod), F32)], axis=0)
    g_w_mod, dcc_part = _mod_bwd(a9, d9, w_mod[0])
    dcc_all, = _all_gather([dcc_part], "ag_c_ctx")
    g_c_ctx = _sum_devices(dcc_all)[0]

    shard = lambda d: jnp.transpose(d.reshape(d.shape[0], N_DEV, -1), (1, 0, 2)).astype(BF16)
    d_gu = r["d_w_gu"]
    send = [shard(_unlayout_w_in(r["d_w_in"])), shard(r["d_w_br_hg"]), shard(r["d_w_br_gla"]),
            r["d_w_out"].reshape(N_DEV, D // N_DEV, D).astype(BF16), shard(d_gu[:, :D_FF]), shard(d_gu[:, D_FF:]),
            r["d_w_down"].reshape(N_DEV, D_FF // N_DEV, D).astype(BF16)]
    recv = _exchange(send, "exchange_grads")
    moms = [(m_w_in, v_w_in), (m_w_br_hg, v_w_br_hg), (m_w_br_gla, v_w_br_gla), (m_w_out, v_w_out),
            (m_w_ff_gate, v_w_ff_gate), (m_w_ff_up, v_w_ff_up), (m_w_ff_down, v_w_ff_down)]
    names = ["w_in", "w_br_hg", "w_br_gla", "w_out", "w_ff_gate", "w_ff_up", "w_ff_down"]
    res = {}
    for nm, w, (m, v), g in zip(names, big, moms, recv):
        res[nm] = [o[None] for o in _adamw(g, w, m[0], v[0], "adamw_" + nm)]
    res["w_mod"] = [o[None] for o in _adamw(g_w_mod[None], w_mod[0], m_w_mod[0], v_w_mod[0], "adamw_w_mod")]

    small = [("c_ctx", c_ctx, m_c_ctx, v_c_ctx, g_c_ctx), ("b_mod", b_mod, m_b_mod, v_b_mod, g_b_mod),
             ("norm_pre1", norm_pre1, m_norm_pre1, v_norm_pre1, g_norms[0]),
             ("norm_post1", norm_post1, m_norm_post1, v_norm_post1, g_norms[1]),
             ("norm_pre2", norm_pre2, m_norm_pre2, v_norm_pre2, g_norms[2]),
             ("norm_post2", norm_post2, m_norm_post2, v_norm_post2, g_norms[3]),
             ("hg_lb", hg_lb, m_hg_lb, v_hg_lb, g_hg_lb), ("hg_onorm", hg_onorm, m_hg_onorm, v_hg_onorm, g_hg_on),
             ("gla_w_gk", gla_w_gk, m_gla_w_gk, v_gla_w_gk, g_wgk), ("gla_b_gk", gla_b_gk, m_gla_b_gk, v_gla_b_gk, g_bgk),
             ("gla_onorm", gla_onorm, m_gla_onorm, v_gla_onorm, g_gla_on)]
    flat = lambda k: jnp.concatenate([s[k].reshape(-1) for s in small]).reshape(-1, DH)
    outs = _adamw(flat(4)[None], flat(1), flat(2), flat(3), "adamw_small")
    off = 0
    for nm, w, _, _, _ in small:
        res[nm] = [o.reshape(-1)[off:off + w.size].reshape(w.shape) for o in outs]
        off += w.size

    order = ["c_ctx", "w_mod", "b_mod", "norm_pre1", "norm_post1", "norm_pre2", "norm_post2", "w_in", "hg_lb",
             "hg_onorm", "gla_w_gk", "gla_b_gk", "gla_onorm", "w_br_hg", "w_br_gla", "w_out", "w_ff_gate", "w_ff_up",
             "w_ff_down"]
    return (loss, grad_x, *[res[n][k] for k in range(4) for n in order])
```

```python
import functools

import jax
import jax.numpy as jnp
from jax import lax
from jax.experimental import pallas as pl
from jax.experimental.pallas import tpu as pltpu

F32 = jnp.float32
BF16 = jnp.bfloat16
HI = lax.Precision.HIGHEST

N_DEV = 8
D = 1024
CTX = 256
HW = 512
DH = 128
NH = 8
D_FF = 2816
EPS = 1e-6
GLA_NORM = 16.0
CHUNK = 64
TR = 256
NCT = CTX // TR
W_IN_COLS = 7168
MAIN0 = 2048
LR0 = 6656
LEVELS = (32, 16, 8)
EXP_CLAMP = 80.0
VMEM_LIMIT = 48 * 1024 * 1024

ADAM_LR, ADAM_B1, ADAM_B2, ADAM_EPS, ADAM_WD, ADAM_STEP = 0.001, 0.9, 0.999, 1e-08, 0.01, 10


def _cp(*sem):
    return pltpu.CompilerParams(dimension_semantics=sem, vmem_limit_bytes=VMEM_LIMIT)


def _sig(x):
    return jax.nn.sigmoid(x)


def _silu(x):
    return x * _sig(x)


def _dsilu(x):
    s = _sig(x)
    return s * (1.0 + x * (1.0 - s))


def _rstd(x):
    return lax.rsqrt(jnp.mean(x * x, axis=-1, keepdims=True) + EPS)


def _rms_bwd(a, y, r):
    return r * (a - y * (r * r) * jnp.mean(a * y, axis=-1, keepdims=True))


def _colsum(x):
    return jnp.sum(x, axis=0, keepdims=True)


def _dot(a, b, dims, precision=None):
    return lax.dot_general(a, b, (dims, ((), ())), preferred_element_type=F32, precision=precision)


NN = ((1,), (0,))
NT = ((1,), (1,))
TN = ((0,), (0,))

SCAN_HEADS_FWD = 4
SCAN_HEADS_BWD = 4


def _split_dot(m, x):
    mb = m.astype(BF16)
    x1 = x.astype(BF16)
    r1 = x - x1.astype(F32)
    x2 = r1.astype(BF16)
    x3 = (r1 - x2.astype(F32)).astype(BF16)
    return _dot(mb, x1, NN) + _dot(mb, x2, NN) + _dot(mb, x3, NN)


def _matmul(a, b, dims, out_dtype, name, tm, tn, tk, a_off=0, m_out=None):
    if dims == NN:
        m, k = a.shape[0], b.shape[0]
        n = b.shape[1]
        a_spec = pl.BlockSpec((tm, tk), lambda i, j, kk: (i, kk + a_off))
        b_spec = pl.BlockSpec((tk, tn), lambda i, j, kk: (kk, j))
    elif dims == NT:
        m, k = a.shape[0], b.shape[1]
        n = b.shape[0]
        a_spec = pl.BlockSpec((tm, tk), lambda i, j, kk: (i, kk + a_off))
        b_spec = pl.BlockSpec((tn, tk), lambda i, j, kk: (j, kk))
    else:
        m, k = (a.shape[1] if m_out is None else m_out), a.shape[0]
        n = b.shape[1]
        a_spec = pl.BlockSpec((tk, tm), lambda i, j, kk: (kk, i + a_off))
        b_spec = pl.BlockSpec((tk, tn), lambda i, j, kk: (kk, j))
    assert m % tm == 0 and n % tn == 0 and k % tk == 0, (name, m, n, k, tm, tn, tk)
    nk = k // tk

    def body(a_ref, b_ref, o_ref, *acc):
        part = _dot(a_ref[...], b_ref[...], dims)
        if nk == 1:
            o_ref[...] = part.astype(o_ref.dtype)
            return
        acc_ref, = acc
        kk = pl.program_id(2)

        @pl.when(kk == 0)
        def _():
            acc_ref[...] = part

        @pl.when(kk > 0)
        def _():
            acc_ref[...] += part

        @pl.when(kk == nk - 1)
        def _():
            o_ref[...] = acc_ref[...].astype(o_ref.dtype)

    return pl.pallas_call(
        body,
        name=name,
        grid=(m // tm, n // tn, nk),
        in_specs=[a_spec, b_spec],
        out_specs=pl.BlockSpec((tm, tn), lambda i, j, kk: (i, j)),
        out_shape=jax.ShapeDtypeStruct((m, n), out_dtype),
        scratch_shapes=[] if nk == 1 else [pltpu.VMEM((tm, tn), F32)],
        compiler_params=_cp("parallel", "parallel", "arbitrary"),
    )(a, b)


def _row(c):
    return pl.BlockSpec((TR, c), lambda i: (i, 0))


def _rowcol(width, cb):
    return pl.BlockSpec((TR, width), lambda i: (i, cb))


def _full(shape):
    return pl.BlockSpec(shape, lambda i: (0,) * len(shape))


def _mod_row(mc_ref, mx_ref, k, is_ctx):
    return jnp.where(is_ctx, mc_ref[k:k + 1, :], mx_ref[k:k + 1, :])


def _acc_row(ref, k, val):
    ref[k:k + 1, :] += val


def _acc_mod(ref, k, is_ctx, val):
    zero = jnp.zeros_like(val)
    ref[k:k + 1, :] += jnp.where(is_ctx, val, zero)
    ref[k + 1:k + 2, :] += jnp.where(is_ctx, zero, val)


def _prenorm(z, nw, modc, modx, i_shift, i_scale, name):
    t = z.shape[0]

    def body(z_ref, nw_ref, mc_ref, mx_ref, h_ref):
        is_ctx = pl.program_id(0) < NCT
        x = z_ref[...]
        n = x * _rstd(x) * nw_ref[...]
        h = n * (1.0 + _mod_row(mc_ref, mx_ref, i_scale, is_ctx)) + _mod_row(mc_ref, mx_ref, i_shift, is_ctx)
        h_ref[...] = h.astype(BF16)

    return pl.pallas_call(
        body, name=name, grid=(t // TR,),
        in_specs=[_row(D), _full((1, D)), _full((8, D)), _full((8, D))],
        out_specs=_row(D),
        out_shape=jax.ShapeDtypeStruct((t, D), BF16),
        compiler_params=_cp("parallel"),
    )(z, nw, modc, modx)


def _hg_lb(lb_ref, d):
    a0 = lb_ref[0, d:d + 1, :]
    a1 = lb_ref[1, d:d + 1, :]
    mx = jnp.maximum(a0, a1)
    e0 = jnp.exp(a0 - mx)
    e1 = jnp.exp(a1 - mx)
    return e0 / (e0 + e1)


def _log_sigmoid(x):
    return jnp.minimum(x, 0.0) - jnp.log(1.0 + jnp.exp(-jnp.abs(x)))


def _gates_fwd(p, hg_lb, wgk, bgk):
    t = p.shape[0]
    seg = lambda j: _rowcol(HW, MAIN0 // HW + j)

    def body(hq_ref, hi_ref, hf_ref, hb_ref, gq_ref, gk_ref, gv_ref, lr_ref, lb_ref, wgk_ref, bgk_ref,
             q_ref, v_ref, kf_ref, kb_ref, gf_ref, gb_ref):
        q_ref[:, :HW] = _silu(hq_ref[...])
        q_ref[:, HW:] = gq_ref[...] * (DH ** -0.5)
        v_ref[:, :HW] = hi_ref[...]
        v_ref[:, HW:] = gv_ref[...]
        xg = _dot(lr_ref[...].astype(BF16), wgk_ref[...], NN) + bgk_ref[...]
        for d, (raw_ref, k_ref, g_ref) in enumerate(((hf_ref, kf_ref, gf_ref), (hb_ref, kb_ref, gb_ref))):
            lbd = _hg_lb(lb_ref, d)
            f = lbd + (1.0 - lbd) * _sig(raw_ref[...])
            k_ref[:, :HW] = 1.0 - f
            k_ref[:, HW:] = gk_ref[...]
            g_ref[:, :HW] = jnp.log(f)
            g_ref[:, HW:] = _log_sigmoid(xg[:, d * HW:(d + 1) * HW]) * (1.0 / GLA_NORM)

    out = jax.ShapeDtypeStruct((t, D), F32)
    return pl.pallas_call(
        body, name="gates_fwd", grid=(t // TR,),
        in_specs=[seg(0), seg(1), seg(2), seg(3), seg(5), seg(6), seg(7), _rowcol(DH, LR0 // DH),
                  _full((2, 2, HW)), _full((DH, D)), _full((1, D))],
        out_specs=[_row(D)] * 6,
        out_shape=[out] * 6,
        compiler_params=_cp("parallel"),
    )(p, p, p, p, p, p, p, p, hg_lb, wgk, bgk)


def _post_fwd(o_fw, o_bw, p, onw):
    t = o_fw.shape[0]

    def body(of_ref, ob_ref, g1_ref, g2_ref, w_ref, y_ref):
        for h in range(NH):
            sl = slice(h * DH, (h + 1) * DH)
            o = of_ref[:, sl] + ob_ref[:, sl]
            g_ref = g1_ref if h < NH // 2 else g2_ref
            gs = slice((h % (NH // 2)) * DH, (h % (NH // 2) + 1) * DH)
            n = o * _rstd(o) * w_ref[:, sl]
            y_ref[:, sl] = (n * _silu(g_ref[:, gs])).astype(BF16)

    return pl.pallas_call(
        body, name="post_fwd", grid=(t // TR,),
        in_specs=[_row(D), _row(D), _rowcol(HW, MAIN0 // HW + 4), _rowcol(HW, MAIN0 // HW + 8), _full((1, D))],
        out_specs=_row(D),
        out_shape=jax.ShapeDtypeStruct((t, D), BF16),
        compiler_params=_cp("parallel"),
    )(o_fw, o_bw, p, p, onw)


def _merge_fwd(p, u1, u2):
    t = p.shape[0]

    def body(g1_ref, g2_ref, u1_ref, u2_ref, m_ref):
        m_ref[...] = (_sig(g1_ref[...]) * u1_ref[...] + _sig(g2_ref[...]) * u2_ref[...]).astype(BF16)

    return pl.pallas_call(
        body, name="merge_fwd", grid=(t // TR,),
        in_specs=[_rowcol(D, 0), _rowcol(D, 1), _row(D), _row(D)],
        out_specs=_row(D),
        out_shape=jax.ShapeDtypeStruct((t, D), BF16),
        compiler_params=_cp("parallel"),
    )(p, p, u1, u2)


def _mid_fwd(z, y1, nw_post, nw_pre, modc, modx):
    t = z.shape[0]

    def body(z_ref, y_ref, wpo_ref, wpr_ref, mc_ref, mx_ref, z1_ref, h_ref):
        is_ctx = pl.program_id(0) < NCT
        y = y_ref[...]
        z1 = z_ref[...] + _mod_row(mc_ref, mx_ref, 2, is_ctx) * (y * _rstd(y) * wpo_ref[...])
        z1_ref[...] = z1
        n = z1 * _rstd(z1) * wpr_ref[...]
        h = n * (1.0 + _mod_row(mc_ref, mx_ref, 4, is_ctx)) + _mod_row(mc_ref, mx_ref, 3, is_ctx)
        h_ref[...] = h.astype(BF16)

    return pl.pallas_call(
        body, name="mid_fwd", grid=(t // TR,),
        in_specs=[_row(D), _row(D), _full((1, D)), _full((1, D)), _full((8, D)), _full((8, D))],
        out_specs=[_row(D), _row(D)],
        out_shape=[jax.ShapeDtypeStruct((t, D), F32), jax.ShapeDtypeStruct((t, D), BF16)],
        compiler_params=_cp("parallel"),
    )(z, y1, nw_post, nw_pre, modc, modx)


def _swiglu_fwd(uv):
    t = uv.shape[0]

    def body(u_ref, v_ref, a_ref):
        a_ref[...] = (_silu(u_ref[...]) * v_ref[...]).astype(BF16)

    return pl.pallas_call(
        body, name="swiglu_fwd", grid=(t // TR,),
        in_specs=[_rowcol(D_FF, 0), _rowcol(D_FF, 1)],
        out_specs=_row(D_FF),
        out_shape=jax.ShapeDtypeStruct((t, D_FF), BF16),
        compiler_params=_cp("parallel"),
    )(uv, uv)


def _swiglu_bwd(uv, da):
    t = uv.shape[0]

    def body(u_ref, v_ref, da_ref, d_ref):
        u = u_ref[...]
        d = da_ref[...]
        d_ref[:, :D_FF] = (d * v_ref[...] * _dsilu(u)).astype(BF16)
        d_ref[:, D_FF:] = (d * _silu(u)).astype(BF16)

    return pl.pallas_call(
        body, name="swiglu_bwd", grid=(t // TR,),
        in_specs=[_rowcol(D_FF, 0), _rowcol(D_FF, 1), _row(D_FF)],
        out_specs=_row(2 * D_FF),
        out_shape=jax.ShapeDtypeStruct((t, 2 * D_FF), BF16),
        compiler_params=_cp("parallel"),
    )(uv, uv, da)


def _final(z1, y2, target, nw, modc, modx):
    t = z1.shape[0]

    def body(z1_ref, y_ref, tg_ref, w_ref, mc_ref, mx_ref, dz_ref, dy_ref, loss_ref, sm_ref):
        i = pl.program_id(0)
        is_ctx = i < NCT

        @pl.when(i == 0)
        def _():
            loss_ref[...] = jnp.zeros_like(loss_ref)
            sm_ref[...] = jnp.zeros_like(sm_ref)

        g = _mod_row(mc_ref, mx_ref, 5, is_ctx)
        y = y_ref[...]
        r = _rstd(y)
        w = w_ref[...]
        yr = y * r
        n = yr * w
        e = z1_ref[...] + g * n - tg_ref[...]
        lat = jnp.where(is_ctx, 0.0, 1.0)
        loss_ref[...] += lat * _colsum(e * e)
        dz = e * (lat / D)
        dz_ref[...] = dz
        _acc_mod(sm_ref, 0, is_ctx, _colsum(dz * n))
        dn = dz * g
        _acc_row(sm_ref, 2, _colsum(dn * yr))
        dy_ref[...] = _rms_bwd(dn * w, y, r).astype(BF16)

    return pl.pallas_call(
        body, name="final", grid=(t // TR,),
        in_specs=[_row(D), _row(D), pl.BlockSpec((TR, D), lambda i: (jnp.maximum(i - NCT, 0), 0)),
                  _full((1, D)), _full((8, D)), _full((8, D))],
        out_specs=[_row(D), _row(D), _full((1, D)), _full((8, D))],
        out_shape=[jax.ShapeDtypeStruct((t, D), F32), jax.ShapeDtypeStruct((t, D), BF16),
                   jax.ShapeDtypeStruct((1, D), F32), jax.ShapeDtypeStruct((8, D), F32)],
        compiler_params=_cp("arbitrary"),
    )(z1, y2, target, nw, modc, modx)


def _mid_bwd(dh2, dz, z, z1, y1, nw_post, nw_pre, modc, modx):
    t = z.shape[0]

    def body(dh_ref, dz_ref, z_ref, z1_ref, y_ref, wpo_ref, wpr_ref, mc_ref, mx_ref, dzo_ref, dy_ref, sm_ref):
        i = pl.program_id(0)
        is_ctx = i < NCT

        @pl.when(i == 0)
        def _():
            sm_ref[...] = jnp.zeros_like(sm_ref)

        dh = dh_ref[...]
        z1 = z1_ref[...]
        r = _rstd(z1)
        zr = z1 * r
        wpr = wpr_ref[...]
        n = zr * wpr
        _acc_mod(sm_ref, 0, is_ctx, _colsum(dh))
        _acc_mod(sm_ref, 2, is_ctx, _colsum(dh * n))
        dn = dh * (1.0 + _mod_row(mc_ref, mx_ref, 4, is_ctx))
        _acc_row(sm_ref, 6, _colsum(dn * zr))
        dz1 = dz_ref[...] + _rms_bwd(dn * wpr, z1, r)
        dzo_ref[...] = dz1
        y = y_ref[...]
        r1 = _rstd(y)
        yr = y * r1
        wpo = wpo_ref[...]
        g = _mod_row(mc_ref, mx_ref, 2, is_ctx)
        _acc_mod(sm_ref, 4, is_ctx, _colsum(dz1 * (yr * wpo)))
        dn1 = dz1 * g
        _acc_row(sm_ref, 7, _colsum(dn1 * yr))
        dy_ref[...] = _rms_bwd(dn1 * wpo, y, r1).astype(BF16)

    return pl.pallas_call(
        body, name="mid_bwd", grid=(t // TR,),
        in_specs=[_row(D)] * 5 + [_full((1, D)), _full((1, D)), _full((8, D)), _full((8, D))],
        out_specs=[_row(D), _row(D), _full((8, D))],
        out_shape=[jax.ShapeDtypeStruct((t, D), F32), jax.ShapeDtypeStruct((t, D), BF16),
                   jax.ShapeDtypeStruct((8, D), F32)],
        compiler_params=_cp("arbitrary"),
    )(dh2, dz, z, z1, y1, nw_post, nw_pre, modc, modx)


def _pre_bwd(dh1, dz, z, nw, modc, modx):
    t = z.shape[0]

    def body(dh_ref, dz_ref, z_ref, w_ref, mc_ref, mx_ref, dzo_ref, sm_ref):
        i = pl.program_id(0)
        is_ctx = i < NCT

        @pl.when(i == 0)
        def _():
            sm_ref[...] = jnp.zeros_like(sm_ref)

        dh = dh_ref[...]
        x = z_ref[...]
        r = _rstd(x)
        xr = x * r
        w = w_ref[...]
        _acc_mod(sm_ref, 0, is_ctx, _colsum(dh))
        _acc_mod(sm_ref, 2, is_ctx, _colsum(dh * (xr * w)))
        dn = dh * (1.0 + _mod_row(mc_ref, mx_ref, 1, is_ctx))
        _acc_row(sm_ref, 4, _colsum(dn * xr))
        dzo_ref[...] = dz_ref[...] + _rms_bwd(dn * w, x, r)

    return pl.pallas_call(
        body, name="pre_bwd", grid=(t // TR,),
        in_specs=[_row(D)] * 3 + [_full((1, D)), _full((8, D)), _full((8, D))],
        out_specs=[_row(D), _full((8, D))],
        out_shape=[jax.ShapeDtypeStruct((t, D), F32), jax.ShapeDtypeStruct((8, D), F32)],
        compiler_params=_cp("arbitrary"),
    )(dh1, dz, z, nw, modc, modx)


def _merge_bwd(dm, p, u1, u2):
    t = dm.shape[0]

    def body(dm_ref, g1_ref, g2_ref, u1_ref, u2_ref, du1_ref, du2_ref, dg_ref):
        dm_ = dm_ref[...]
        s1 = _sig(g1_ref[...])
        s2 = _sig(g2_ref[...])
        du1_ref[...] = (dm_ * s1).astype(BF16)
        du2_ref[...] = (dm_ * s2).astype(BF16)
        dg_ref[:, :D] = (dm_ * u1_ref[...] * s1 * (1.0 - s1)).astype(BF16)
        dg_ref[:, D:] = (dm_ * u2_ref[...] * s2 * (1.0 - s2)).astype(BF16)

    return pl.pallas_call(
        body, name="merge_bwd", grid=(t // TR,),
        in_specs=[_row(D), _rowcol(D, 0), _rowcol(D, 1), _row(D), _row(D)],
        out_specs=[_row(D), _row(D), _row(2 * D)],
        out_shape=[jax.ShapeDtypeStruct((t, D), BF16), jax.ShapeDtypeStruct((t, D), BF16),
                   jax.ShapeDtypeStruct((t, 2 * D), BF16)],
        compiler_params=_cp("parallel"),
    )(dm, p, p, u1, u2)


def _post_bwd(dy_hg, dy_gla, o_fw, o_bw, p, onw):
    t = o_fw.shape[0]

    def body(d1_ref, d2_ref, of_ref, ob_ref, g1_ref, g2_ref, w_ref, do_ref, dg_ref, sm_ref):
        @pl.when(pl.program_id(0) == 0)
        def _():
            sm_ref[...] = jnp.zeros_like(sm_ref)

        for h in range(NH):
            sl = slice(h * DH, (h + 1) * DH)
            gs = slice((h % (NH // 2)) * DH, (h % (NH // 2) + 1) * DH)
            g_ref, d_ref = (g1_ref, d1_ref) if h < NH // 2 else (g2_ref, d2_ref)
            o = of_ref[:, sl] + ob_ref[:, sl]
            r = _rstd(o)
            orr = o * r
            w = w_ref[:, sl]
            gt = g_ref[:, gs]
            dy = d_ref[:, gs]
            dg_ref[:, sl] = (dy * (orr * w) * _dsilu(gt)).astype(BF16)
            dn = dy * _silu(gt)
            sm_ref[0:1, sl] += _colsum(dn * orr)
            do_ref[:, sl] = _rms_bwd(dn * w, o, r)

    return pl.pallas_call(
        body, name="post_bwd", grid=(t // TR,),
        in_specs=[_row(HW), _row(HW), _row(D), _row(D), _rowcol(HW, MAIN0 // HW + 4), _rowcol(HW, MAIN0 // HW + 8),
                  _full((1, D))],
        out_specs=[_row(D), _row(D), _full((8, D))],
        out_shape=[jax.ShapeDtypeStruct((t, D), F32), jax.ShapeDtypeStruct((t, D), BF16),
                   jax.ShapeDtypeStruct((8, D), F32)],
        compiler_params=_cp("arbitrary"),
    )(dy_hg, dy_gla, o_fw, o_bw, p, p, onw)


def _gates_bwd(p, hg_lb, wgk, bgk, dgm, dgo, dq_f, dq_b, dv_f, dv_b, dk_f, dk_b, dg_f, dg_b):
    t = p.shape[0]
    seg = lambda j: _rowcol(HW, MAIN0 // HW + j)

    def body(hq_ref, hf_ref, hb_ref, lr_ref, lb_ref, wgk_ref, bgk_ref, dgm_ref, dgo_ref,
             dqf_ref, dqb_ref, dvf_ref, dvb_ref, dkf_ref, dkb_ref, dgf_ref, dgb_ref,
             dp_ref, dlb_ref, dw_ref, db_ref):
        @pl.when(pl.program_id(0) == 0)
        def _():
            dlb_ref[...] = jnp.zeros_like(dlb_ref)
            dw_ref[...] = jnp.zeros_like(dw_ref)
            db_ref[...] = jnp.zeros_like(db_ref)

        c0 = MAIN0

        def put(j, val):
            dp_ref[:, c0 + j * HW:c0 + (j + 1) * HW] = val.astype(BF16)

        dp_ref[:, :MAIN0] = dgm_ref[...]
        dq = dqf_ref[...] + dqb_ref[...]
        dv = dvf_ref[...] + dvb_ref[...]
        put(0, dq[:, :HW] * _dsilu(hq_ref[...]))
        put(1, dv[:, :HW])
        put(5, dq[:, HW:] * (DH ** -0.5))
        put(7, dv[:, HW:])
        put(6, dkf_ref[:, HW:] + dkb_ref[:, HW:])
        dp_ref[:, c0 + 4 * HW:c0 + 5 * HW] = dgo_ref[:, :HW]
        dp_ref[:, c0 + 8 * HW:c0 + 9 * HW] = dgo_ref[:, HW:]
        lr = lr_ref[...].astype(BF16)
        xg = _dot(lr, wgk_ref[...], NN) + bgk_ref[...]
        dxg = []
        for d, (raw_ref, dk_ref, dg_ref) in enumerate(((hf_ref, dkf_ref, dgf_ref), (hb_ref, dkb_ref, dgb_ref))):
            lbd = _hg_lb(lb_ref, d)
            s = _sig(raw_ref[...])
            f = lbd + (1.0 - lbd) * s
            df = dg_ref[:, :HW] / f - dk_ref[:, :HW]
            put(2 + d, df * (1.0 - lbd) * s * (1.0 - s))
            dlb_ref[d:d + 1, :] += _colsum(df * (1.0 - s)) * (lbd * (1.0 - lbd))
            dxg.append(dg_ref[:, HW:] * (1.0 / GLA_NORM) * _sig(-xg[:, d * HW:(d + 1) * HW]))
        dxg = jnp.concatenate(dxg, axis=1)
        db_ref[0:1, :] += _colsum(dxg)
        dxg_b = dxg.astype(BF16)
        dw_ref[...] += _dot(lr, dxg_b, TN)
        dp_ref[:, LR0:LR0 + DH] = _dot(dxg_b, wgk_ref[...], NT).astype(BF16)
        dp_ref[:, LR0 + DH:] = jnp.zeros((TR, W_IN_COLS - LR0 - DH), BF16)

    return pl.pallas_call(
        body, name="gates_bwd", grid=(t // TR,),
        in_specs=[seg(0), seg(2), seg(3), _rowcol(DH, LR0 // DH), _full((2, 2, HW)), _full((DH, D)), _full((1, D)),
                  _row(2 * D), _row(D)] + [_row(D)] * 8,
        out_specs=[_row(W_IN_COLS), _full((8, HW)), _full((DH, D)), _full((8, D))],
        out_shape=[jax.ShapeDtypeStruct((t, W_IN_COLS), BF16), jax.ShapeDtypeStruct((8, HW), F32),
                   jax.ShapeDtypeStruct((DH, D), F32), jax.ShapeDtypeStruct((8, D), F32)],
        compiler_params=_cp("arbitrary"),
    )(p, p, p, p, hg_lb, wgk, bgk, dgm, dgo, dq_f, dq_b, dv_f, dv_b, dk_f, dk_b, dg_f, dg_b)


def _scan_consts(rev):
    r = lax.broadcasted_iota(jnp.int32, (CHUNK, CHUNK), 0)
    u = lax.broadcasted_iota(jnp.int32, (CHUNK, CHUNK), 1)
    rp = lax.broadcasted_iota(jnp.int32, (CHUNK, 1), 0)
    if rev:
        r, u, rp = CHUNK - 1 - r, CHUNK - 1 - u, CHUNK - 1 - rp
    tri = jnp.where(u <= r, 1.0, 0.0).astype(F32)
    tri_t = jnp.where(r <= u, 1.0, 0.0).astype(F32)
    lv = []
    for b in LEVELS:
        sh = b.bit_length() - 1
        pair = ((r >> sh) == (u >> sh) + 1) & (((u >> sh) & 1) == 0)
        pair_t = ((u >> sh) == (r >> sh) + 1) & (((r >> sh) & 1) == 0)
        tside = ((rp >> sh) & 1) == 1
        lv.append((pair, pair_t, tside))
    bd = LEVELS[-1].bit_length() - 1
    diag = ((r >> bd) == (u >> bd)) & (u <= r)
    diag_t = ((r >> bd) == (u >> bd)) & (r <= u)
    return tri, tri_t, lv, diag, diag_t


def _row_of(pos, rev):
    return CHUNK - 1 - pos if rev else pos


def _chunk_terms(cum, b_scr, consts, rev):
    _, _, lv, _, _ = consts
    terms = []
    for b, (_, _, tside) in zip(LEVELS, lv):
        pieces = []
        for j in range(CHUNK // (2 * b)):
            row = _row_of(2 * b * j + b - 1, rev)
            pieces.append(jnp.broadcast_to(b_scr[row:row + 1, :], (2 * b, DH)))
        if rev:
            pieces = pieces[::-1]
        bnd = pieces[0] if len(pieces) == 1 else jnp.concatenate(pieces, axis=0)
        w = jnp.exp(jnp.minimum(jnp.where(tside, cum - bnd, bnd - cum), 0.0))
        wq = jnp.where(tside, w, 0.0)
        wk = jnp.where(tside, 0.0, w)
        terms.append((wq, wk))
    b = LEVELS[-1]
    pieces = []
    for j in range(CHUNK // b):
        if j == 0:
            pieces.append(jnp.zeros((b, DH), F32))
        else:
            row = _row_of(b * j - 1, rev)
            pieces.append(jnp.broadcast_to(b_scr[row:row + 1, :], (b, DH)))
    if rev:
        pieces = pieces[::-1]
    start = jnp.concatenate(pieces, axis=0)
    wq = jnp.exp(jnp.minimum(cum - start, 0.0))
    wk = jnp.exp(jnp.minimum(start - cum, EXP_CLAMP))
    terms.append((wq, wk))
    return terms


def _run_staged(units):
    live = list(units)
    while live:
        nxt = []
        for u in live:
            try:
                next(u)
                nxt.append(u)
            except StopIteration:
                pass
        live = nxt


SCAN_TB = 256
SCAN_CB = SCAN_TB // CHUNK


def _block_order(i, ntb, rev):
    nctx = CTX // SCAN_TB
    if not rev:
        return i
    return jnp.where(i < nctx, nctx - 1 - i, ntb - 1 - (i - nctx))


def _chunk_in_block(j, rev):
    return SCAN_CB - 1 - j if rev else j


def _scan_fwd(q, k, v, g, rev):
    t = q.shape[0]
    nc = t // CHUNK
    hpb = SCAN_HEADS_FWD

    def body(q_ref, k_ref, v_ref, g_ref, o_ref, st_ref, s_scr, b_scr):
        consts = _scan_consts(rev)
        _, _, lv, diag, _ = consts
        masks = [pair for pair, _, _ in lv] + [diag]

        @pl.when(pl.program_id(1) == 0)
        def _():
            s_scr[...] = jnp.zeros_like(s_scr)

        tri = consts[0]
        state = {hh: s_scr[hh] for hh in range(hpb)}

        def unit(hh, j):
            sl = slice(hh * DH, (hh + 1) * DH)
            c = _chunk_in_block(j, rev)
            rows = slice(c * CHUNK, (c + 1) * CHUNK)
            b_ref = b_scr.at[hh * SCAN_CB + j]
            qc, kc, vc, gc = q_ref[rows, sl], k_ref[rows, sl], v_ref[rows, sl], g_ref[rows, sl]
            cum = _split_dot(tri, gc)
            b_ref[...] = cum
            yield
            terms = _chunk_terms(cum, b_ref, consts, rev)
            ops = [((qc * wq).astype(BF16), (kc * wk).astype(BF16)) for wq, wk in terms]
            tot = _colsum(gc)
            qe = (qc * jnp.exp(cum)).astype(BF16)
            ke = (kc * jnp.exp(tot - cum)).astype(BF16)
            vb = vc.astype(BF16)
            yield
            scs = [_dot(qt, kt, NT) for qt, kt in ops]
            kv = _dot(vb, ke, TN)
            yield
            a = jnp.zeros((CHUNK, CHUNK), F32)
            for sc, m in zip(scs, masks):
                a = a + jnp.where(m, sc, 0.0)
            o_intra = _dot(a.astype(BF16), vb, NN)
            yield
            st = state[hh]
            st_ref[hh, c] = st
            o_ref[rows, sl] = o_intra + _dot(qe, st.astype(BF16), NT)
            state[hh] = st * jnp.exp(tot) + kv
            yield

        _run_staged([unit(hh, j) for hh in range(hpb) for j in range(SCAN_CB)])
        for hh in range(hpb):
            s_scr[hh] = state[hh]

    ntb = t // SCAN_TB
    col = pl.BlockSpec((SCAN_TB, hpb * DH), lambda h, i: (_block_order(i, ntb, rev), h))
    return pl.pallas_call(
        body, name="scan_fwd_" + ("bw" if rev else "fw"), grid=(NH // hpb, ntb),
        in_specs=[col] * 4,
        out_specs=[col, pl.BlockSpec((hpb, SCAN_CB, DH, DH), lambda h, i: (h, _block_order(i, ntb, rev), 0, 0))],
        out_shape=[jax.ShapeDtypeStruct((t, D), F32), jax.ShapeDtypeStruct((NH, nc, DH, DH), F32)],
        scratch_shapes=[pltpu.VMEM((hpb, DH, DH), F32), pltpu.VMEM((hpb * SCAN_CB, CHUNK, DH), F32)],
        compiler_params=_cp("parallel", "arbitrary"),
    )(q, k, v, g)


def _scan_bwd(q, k, v, g, do, states, rev):
    t = q.shape[0]
    nc = t // CHUNK
    hpb = SCAN_HEADS_BWD

    def body(q_ref, k_ref, v_ref, g_ref, do_ref, st_ref, dq_ref, dk_ref, dv_ref, dg_ref, ds_scr, b_scr):
        consts = _scan_consts(rev)
        _, tri_t, lv, diag, diag_t = consts
        masks = [(pair, pair_t) for pair, pair_t, _ in lv] + [(diag, diag_t)]
        @pl.when(pl.program_id(1) == 0)
        def _():
            ds_scr[...] = jnp.zeros_like(ds_scr)

        tri = consts[0]
        dstate = {hh: ds_scr[hh] for hh in range(hpb)}

        def unit(hh, jj):
            sl = slice(hh * DH, (hh + 1) * DH)
            c = _chunk_in_block(SCAN_CB - 1 - jj, rev)
            rows = slice(c * CHUNK, (c + 1) * CHUNK)
            b_ref = b_scr.at[hh * SCAN_CB + jj]
            qc, kc, vc, gc = q_ref[rows, sl], k_ref[rows, sl], v_ref[rows, sl], g_ref[rows, sl]
            dob = do_ref[rows, sl].astype(BF16)
            vb = vc.astype(BF16)
            cum = _split_dot(tri, gc)
            b_ref[...] = cum
            da = _dot(dob, vb, NT)
            da_t = _dot(vb, dob, NT)
            yield
            terms = _chunk_terms(cum, b_ref, consts, rev)
            ops = [((qc * wq).astype(BF16), (kc * wk).astype(BF16)) for wq, wk in terms]
            tot = _colsum(gc)
            e_tot = jnp.exp(tot)
            e_b = jnp.exp(cum)
            e_t = jnp.exp(tot - cum)
            qeb = (qc * e_b).astype(BF16)
            keb = (kc * e_t).astype(BF16)
            dal = [(jnp.where(m, da, 0.0).astype(BF16), jnp.where(m_t, da_t, 0.0).astype(BF16)) for m, m_t in masks]
            yield
            ats = [_dot(ktb, qtb, NT) for qtb, ktb in ops]
            dqts = [_dot(d, ktb, NN) for (d, _), (_, ktb) in zip(dal, ops)]
            dkts = [_dot(d_t, qtb, NN) for (_, d_t), (qtb, _) in zip(dal, ops)]
            qd = _dot(dob, qeb, TN)
            yield
            a_t = jnp.zeros((CHUNK, CHUNK), F32)
            dq = jnp.zeros((CHUNK, DH), F32)
            dk = jnp.zeros((CHUNK, DH), F32)
            db = jnp.zeros((CHUNK, DH), F32)
            for at, dqt, dkt, (wq, wk), (qtb, ktb), (_, m_t) in zip(ats, dqts, dkts, terms, ops, masks):
                a_t = a_t + jnp.where(m_t, at, 0.0)
                dq = dq + dqt * wq
                dk = dk + dkt * wk
                db = db + dqt * qtb.astype(F32) - dkt * ktb.astype(F32)
            dv_intra = _dot(a_t.astype(BF16), dob, NN)
            st = st_ref[hh, c]
            stb = st.astype(BF16)
            dqe = _dot(dob, stb, NN)
            yield
            dst = dstate[hh]
            dstb = dst.astype(BF16)
            dstate[hh] = dst * e_tot + qd
            dv_ref[rows, sl] = dv_intra + _dot(keb, dstb, NT)
            dke = _dot(vb, dstb, NN)
            yield
            qe = qeb.astype(F32)
            ke = keb.astype(F32)
            dq_ref[rows, sl] = dq + dqe * e_b
            dk_ref[rows, sl] = dk + dke * e_t
            db = db + dqe * qe - dke * ke
            dtot = _colsum(dstb.astype(F32) * stb.astype(F32)) * e_tot + _colsum(dke * ke)
            dg_ref[rows, sl] = _split_dot(tri_t, db) + dtot
            yield

        _run_staged([unit(hh, jj) for hh in range(hpb) for jj in range(SCAN_CB)])
        for hh in range(hpb):
            ds_scr[hh] = dstate[hh]

    ntb = t // SCAN_TB
    blk = lambda i: _block_order(ntb - 1 - i, ntb, rev)
    col = pl.BlockSpec((SCAN_TB, hpb * DH), lambda h, i: (blk(i), h))
    out = jax.ShapeDtypeStruct((t, D), F32)
    return pl.pallas_call(
        body, name="scan_bwd_" + ("bw" if rev else "fw"), grid=(NH // hpb, ntb),
        in_specs=[col] * 5 + [pl.BlockSpec((hpb, SCAN_CB, DH, DH), lambda h, i: (h, blk(i), 0, 0))],
        out_specs=[col] * 4,
        out_shape=[out] * 4,
        scratch_shapes=[pltpu.VMEM((hpb, DH, DH), F32), pltpu.VMEM((hpb * SCAN_CB, CHUNK, DH), F32)],
        compiler_params=_cp("parallel", "arbitrary"),
    )(q, k, v, g, do, states)


W_IN_REF = 6688
GATE0 = 4640
LRW = 32


def _layout_w_in(w):
    return jnp.concatenate([w[:, GATE0:], w[:, :GATE0 - LRW], w[:, GATE0 - LRW:GATE0],
                            jnp.zeros((w.shape[0], W_IN_COLS - W_IN_REF), w.dtype)], axis=1)


def _unlayout_w_in(d):
    return jnp.concatenate([d[:, MAIN0:LR0 + LRW], d[:, :MAIN0]], axis=1)


def _layout_wgk(w):
    r = w.shape[1]
    top = jnp.concatenate([w[0], jnp.zeros_like(w[0])], axis=1)
    bot = jnp.concatenate([jnp.zeros_like(w[1]), w[1]], axis=1)
    return jnp.concatenate([top, bot, jnp.zeros((DH - 2 * r, D), w.dtype)], axis=0)


def _unlayout_wgk(d, r=16):
    return jnp.stack([d[:r, :HW], d[r:2 * r, HW:]])


def _local_step(z, target, modc, modx, norms, onw, hg_lb, wgk, bgk, w_in, w_br_hg, w_br_gla, w_out, w_gu, w_down):
    n_pre1, n_post1, n_pre2, n_post2 = norms
    t = z.shape[0]
    tm = 768 if t % 768 == 0 else 256
    h1 = _prenorm(z, n_pre1, modc, modx, 0, 1, "prenorm1")
    p = _matmul(h1, w_in, NN, F32, "mm_in", tm, 512, D)
    q, v, k_f, k_b, g_f, g_b = _gates_fwd(p, hg_lb, wgk, bgk)
    o_f, st_f = _scan_fwd(q, k_f, v, g_f, False)
    o_b, st_b = _scan_fwd(q, k_b, v, g_b, True)
    y = _post_fwd(o_f, o_b, p, onw)
    u1 = _matmul(y, w_br_hg, NN, F32, "mm_br_hg", tm, 512, HW, a_off=0)
    u2 = _matmul(y, w_br_gla, NN, F32, "mm_br_gla", tm, 512, HW, a_off=1)
    merged = _merge_fwd(p, u1, u2)
    y1 = _matmul(merged, w_out, NN, F32, "mm_out", tm, 512, D)
    z1, h2 = _mid_fwd(z, y1, n_post1, n_pre2, modc, modx)
    uv = _matmul(h2, w_gu, NN, F32, "mm_gu", tm, 512, D)
    act = _swiglu_fwd(uv)
    y2 = _matmul(act, w_down, NN, F32, "mm_down", tm, 512, D_FF // 2)
    dz, dy2, loss_vec, sm_final = _final(z1, y2, target, n_post2, modc, modx)
    dact = _matmul(dy2, w_down, NT, F32, "mm_down_dx", tm, D_FF // 2, D)
    d_w_down = _matmul(act, dy2, TN, F32, "mm_down_dw", D_FF // 2, 512, t)
    duv = _swiglu_bwd(uv, dact)
    dh2 = _matmul(duv, w_gu, NT, F32, "mm_gu_dx", tm, 512, D_FF // 2)
    d_w_gu = _matmul(h2, duv, TN, F32, "mm_gu_dw", 512, 512, t)
    dz, dy1, sm_mid = _mid_bwd(dh2, dz, z, z1, y1, n_post1, n_pre2, modc, modx)
    dmerged = _matmul(dy1, w_out, NT, F32, "mm_out_dx", tm, 512, D)
    d_w_out = _matmul(merged, dy1, TN, F32, "mm_out_dw", 512, 512, t)
    du1, du2, dgm = _merge_bwd(dmerged, p, u1, u2)
    dy_hg = _matmul(du1, w_br_hg, NT, F32, "mm_br_hg_dx", tm, HW, D)
    dy_gla = _matmul(du2, w_br_gla, NT, F32, "mm_br_gla_dx", tm, HW, D)
    d_w_br_hg = _matmul(y, du1, TN, F32, "mm_br_hg_dw", HW, 512, t, a_off=0, m_out=HW)
    d_w_br_gla = _matmul(y, du2, TN, F32, "mm_br_gla_dw", HW, 512, t, a_off=1, m_out=HW)
    do, dgo, sm_post = _post_bwd(dy_hg, dy_gla, o_f, o_b, p, onw)
    dq_f, dk_f, dv_f, dg_f = _scan_bwd(q, k_f, v, g_f, do, st_f, False)
    dq_b, dk_b, dv_b, dg_b = _scan_bwd(q, k_b, v, g_b, do, st_b, True)
    dp, d_lb, d_wgk, d_bgk = _gates_bwd(p, hg_lb, wgk, bgk, dgm, dgo, dq_f, dq_b, dv_f, dv_b, dk_f, dk_b, dg_f, dg_b)
    dh1 = _matmul(dp, w_in, NT, F32, "mm_in_dx", tm, 512, 1024)
    d_w_in = _matmul(h1, dp, TN, F32, "mm_in_dw", 512, 512, t)
    dz, sm_pre = _pre_bwd(dh1, dz, z, n_pre1, modc, modx)
    return dict(loss_vec=loss_vec, dz=dz, sm_final=sm_final, sm_mid=sm_mid, sm_post=sm_post, sm_pre=sm_pre,
                d_lb=d_lb, d_wgk=d_wgk, d_bgk=d_bgk, d_w_in=d_w_in, d_w_br_hg=d_w_br_hg, d_w_br_gla=d_w_br_gla,
                d_w_out=d_w_out, d_w_gu=d_w_gu, d_w_down=d_w_down)


MESH = pl.DeviceIdType.MESH
ANY = pl.BlockSpec(memory_space=pl.ANY)
N_REL = N_DEV - 1


def _place():
    return lax.axis_index("x"), lax.axis_index("y"), lax.axis_index("c")


def _slot(p):
    return 4 * p[0] + 2 * p[1] + p[2]


def _all_gather(arrays, name):
    n = len(arrays)

    def body(*refs):
        ins, outs = refs[:n], refs[n:2 * n]
        send_sems, recv_sems, local_sems = refs[2 * n:]
        x, y, c = _place()
        me, sibling = (x, y, c), (x, y, 1 - c)
        chips = [(1 - x, y), (x, 1 - y), (1 - x, 1 - y)]

        def copy(a, k, block, to, src=None):
            dst = outs[a].at[_slot(block)]
            return pltpu.make_async_remote_copy(
                src_ref=dst if src is None else src, dst_ref=dst,
                send_sem=send_sems.at[N_REL * a + k], recv_sem=recv_sems.at[N_REL * a + k],
                device_id=to, device_id_type=MESH)

        mine = [pltpu.make_async_copy(ins[a], outs[a].at[_slot(me)], local_sems.at[a]) for a in range(n)]
        for cp in mine:
            cp.start()
        first = []
        for a in range(n):
            first.append(copy(a, 0, me, sibling, src=ins[a]))
            first += [copy(a, 1 + j, me, (*chip, c), src=ins[a]) for j, chip in enumerate(chips)]
        for cp in first:
            cp.start()
        passed = []
        for j, chip in enumerate(chips):
            for a in range(n):
                copy(a, 1 + j, (*chip, c), me).wait_recv()
                fwd = copy(a, 4 + j, (*chip, c), sibling)
                fwd.start()
                passed.append(fwd)
        for a in range(n):
            copy(a, 0, sibling, me).wait_recv()
        for j, chip in enumerate(chips):
            for a in range(n):
                copy(a, 4 + j, (*chip, 1 - c), me).wait_recv()
        for cp in first + passed:
            cp.wait_send()
        for cp in mine:
            cp.wait()

    return pl.pallas_call(
        body, name=name,
        in_specs=[ANY] * n, out_specs=[ANY] * n,
        out_shape=[jax.ShapeDtypeStruct((N_DEV,) + a.shape, a.dtype) for a in arrays],
        scratch_shapes=[pltpu.SemaphoreType.DMA((N_REL * n,)), pltpu.SemaphoreType.DMA((N_REL * n,)),
                        pltpu.SemaphoreType.DMA((n,))],
    )(*arrays)


def _exchange(arrays, name):
    n = len(arrays)

    def body(*refs):
        ins, outs = refs[:n], refs[n:2 * n]
        send_sems, recv_sems, local_sems = refs[2 * n:]
        x, y, c = _place()
        me = _slot((x, y, c))
        mine = [pltpu.make_async_copy(ins[a].at[me], outs[a].at[me], local_sems.at[a]) for a in range(n)]
        for cp in mine:
            cp.start()
        copies = []
        for a in range(n):
            for k in range(1, N_DEV):
                flip = lambda v, bit: 1 - v if bit else v
                peer = (flip(x, k & 4), flip(y, k & 2), flip(c, k & 1))
                copies.append(pltpu.make_async_remote_copy(
                    src_ref=ins[a].at[_slot(peer)], dst_ref=outs[a].at[me],
                    send_sem=send_sems.at[N_REL * a + k - 1], recv_sem=recv_sems.at[N_REL * a + k - 1],
                    device_id=peer, device_id_type=MESH))
                copies[-1].start()
        i = 0
        for a in range(n):
            for k in range(1, N_DEV):
                flip = lambda v, bit: 1 - v if bit else v
                peer = (flip(x, k & 4), flip(y, k & 2), flip(c, k & 1))
                pltpu.make_async_remote_copy(
                    src_ref=ins[a].at[_slot(peer)], dst_ref=outs[a].at[_slot(peer)],
                    send_sem=send_sems.at[N_REL * a + k - 1], recv_sem=recv_sems.at[N_REL * a + k - 1],
                    device_id=peer, device_id_type=MESH).wait_recv()
                i += 1
        for cp in copies:
            cp.wait_send()
        for cp in mine:
            cp.wait()

    return pl.pallas_call(
        body, name=name,
        in_specs=[ANY] * n, out_specs=[ANY] * n,
        out_shape=[jax.ShapeDtypeStruct(a.shape, a.dtype) for a in arrays],
        scratch_shapes=[pltpu.SemaphoreType.DMA((N_REL * n,)), pltpu.SemaphoreType.DMA((N_REL * n,)),
                        pltpu.SemaphoreType.DMA((n,))],
    )(*arrays)


def _mod_fwd(a, w, b):
    def body(a_ref, w_ref, b_ref, o_ref):
        o_ref[...] = _dot(_silu(a_ref[...]), w_ref[...], NN, precision=HI) + b_ref[...]

    return pl.pallas_call(
        body, name="mod_fwd", out_shape=jax.ShapeDtypeStruct((a.shape[0], w.shape[1]), F32),
        compiler_params=pltpu.CompilerParams(vmem_limit_bytes=VMEM_LIMIT),
    )(a, w, b)


def _mod_bwd(a, d, w):
    def body(a_ref, d_ref, w_ref, dw_ref, dc_ref):
        av = a_ref[...]
        dv = d_ref[...]
        dw_ref[...] = _dot(_silu(av), dv, TN, precision=HI)
        da = _dot(dv[0:8, :], w_ref[...], NT, precision=HI) * _dsilu(av[0:8, :])
        row = lax.broadcasted_iota(jnp.int32, da.shape, 0)
        dc_ref[...] = jnp.where(row == 0, da, 0.0)

    return pl.pallas_call(
        body, name="mod_bwd",
        out_shape=[jax.ShapeDtypeStruct(w.shape, F32), jax.ShapeDtypeStruct((8, w.shape[0]), F32)],
        compiler_params=pltpu.CompilerParams(vmem_limit_bytes=VMEM_LIMIT),
    )(a, d, w)


def _sum_devices(g):
    def body(g_ref, o_ref):
        acc = g_ref[0]
        for i in range(1, g.shape[0]):
            acc = acc + g_ref[i]
        o_ref[...] = acc

    return pl.pallas_call(body, name="sum_devices_%d" % g.shape[1],
                          out_shape=jax.ShapeDtypeStruct(g.shape[1:], F32))(g)


def _adam_rows(r, c, n):
    budget = 6 * 1024 * 1024
    best = None
    for tr in range(16, r + 1, 16):
        if r % tr == 0 and tr * c * (2 * n + 28) <= budget:
            best = tr
    return best if best is not None else r


def _adamw(g, w, m, v, name):
    n, r, c = g.shape
    tr = _adam_rows(r, c, n)
    bc1 = 1.0 - ADAM_B1 ** ADAM_STEP
    bc2 = 1.0 - ADAM_B2 ** ADAM_STEP

    def body(g_ref, w_ref, m_ref, v_ref, go_ref, d_ref, mo_ref, vo_ref):
        grad = g_ref[0].astype(F32)
        for i in range(1, n):
            grad = grad + g_ref[i].astype(F32)
        go_ref[...] = grad
        m_new = ADAM_B1 * m_ref[...] + (1.0 - ADAM_B1) * grad
        v_new = ADAM_B2 * v_ref[...] + (1.0 - ADAM_B2) * (grad * grad)
        mo_ref[...] = m_new
        vo_ref[...] = v_new
        d_ref[...] = -ADAM_LR * ((m_new / bc1) / (jnp.sqrt(v_new / bc2) + ADAM_EPS) + ADAM_WD * w_ref[...])

    blk = pl.BlockSpec((tr, c), lambda i: (i, 0))
    out = jax.ShapeDtypeStruct((r, c), F32)
    return pl.pallas_call(
        body, name=name, grid=(r // tr,),
        in_specs=[pl.BlockSpec((n, tr, c), lambda i: (0, i, 0)), blk, blk, blk],
        out_specs=[blk] * 4, out_shape=[out] * 4,
        compiler_params=_cp("parallel"),
    )(g, w, m, v)


def kernel(x, c, ctx, c_ctx, w_mod, b_mod, norm_pre1, norm_post1, norm_pre2, norm_post2, w_in, hg_lb, hg_onorm, gla_w_gk, gla_b_gk, gla_onorm, w_br_hg, w_br_gla, w_out, w_ff_gate, w_ff_up, w_ff_down, loss_target, m_c_ctx, m_w_mod, m_b_mod, m_norm_pre1, m_norm_post1, m_norm_pre2, m_norm_post2, m_w_in, m_hg_lb, m_hg_onorm, m_gla_w_gk, m_gla_b_gk, m_gla_onorm, m_w_br_hg, m_w_br_gla, m_w_out, m_w_ff_gate, m_w_ff_up, m_w_ff_down, v_c_ctx, v_w_mod, v_b_mod, v_norm_pre1, v_norm_post1, v_norm_pre2, v_norm_post2, v_w_in, v_hg_lb, v_hg_onorm, v_gla_w_gk, v_gla_b_gk, v_gla_onorm, v_w_br_hg, v_w_br_gla, v_w_out, v_w_ff_gate, v_w_ff_up, v_w_ff_down):
    xi, yi, ci = lax.axis_index("x"), lax.axis_index("y"), lax.axis_index("c")
    me = 4 * xi + 2 * yi + ci
    t = CTX + x.shape[1]

    c_all, lb_g, wgk_g, bgk_g = _all_gather([c, hg_lb, gla_w_gk[0], gla_b_gk[0]], "ag_small")
    big = [w_in[0], w_br_hg[0], w_br_gla[0], w_out[0], w_ff_gate[0], w_ff_up[0], w_ff_down[0]]
    g_in, g_brh, g_brg, g_out, g_gate, g_up, g_down = _all_gather([w.astype(BF16) for w in big], "ag_weights")
    cols = lambda g: jnp.transpose(g, (1, 0, 2)).reshape(g.shape[1], N_DEV * g.shape[2])
    w_in_k = _layout_w_in(cols(g_in))
    w_gu_k = jnp.concatenate([cols(g_gate), cols(g_up)], axis=1)
    w_down_k = g_down.reshape(D_FF, D)
    w_out_k = g_out.reshape(D, D)
    hg_lb_full = jnp.transpose(lb_g, (1, 2, 0, 3)).reshape(2, 2, HW)
    wgk_k = _layout_wgk(jnp.transpose(wgk_g, (1, 2, 0, 3)).reshape(2, 16, HW)).astype(BF16)
    bgk_k = jnp.transpose(bgk_g, (1, 0, 2)).reshape(1, D)
    onw = jnp.concatenate([jnp.tile(hg_onorm, (1, NH // 2)), jnp.tile(gla_onorm, (1, NH // 2))], axis=1)

    n_mod = w_mod.shape[2]
    a9 = jnp.concatenate([c_ctx[None], c_all[:, 0], jnp.zeros((16 - 1 - N_DEV, D), F32)], axis=0)
    b_loc = lax.dynamic_slice(b_mod, (0, me * n_mod), (1, n_mod))
    s_loc = _mod_fwd(a9, w_mod[0], b_loc)
    s_all, = _all_gather([s_loc], "ag_mod")
    mod_all = jnp.transpose(s_all, (1, 0, 2)).reshape(16, N_DEV * n_mod)
    pad8 = lambda m: jnp.concatenate([m.reshape(6, D), jnp.zeros((2, D), F32)], axis=0)
    modc = pad8(mod_all[0])
    modx = pad8(lax.dynamic_slice(mod_all, (1 + me, 0), (1, N_DEV * n_mod))[0])

    z = jnp.concatenate([ctx[0], x[0]], axis=0)
    norms = (norm_pre1, norm_post1, norm_pre2, norm_post2)
    r = _local_step(z, loss_target[0], modc, modx, norms, onw, hg_lb_full, wgk_k, bgk_k,
                    w_in_k, cols(g_brh), cols(g_brg), w_out_k, w_gu_k, w_down_k)
    loss = lax.psum((0.5 / D) * jnp.sum(r["loss_vec"]), ("x", "y", "c"))
    grad_x = r["dz"][CTX:][None]

    sm_pre, sm_mid, sm_fin = r["sm_pre"], r["sm_mid"], r["sm_final"]
    dmodc = jnp.stack([sm_pre[0], sm_pre[2], sm_mid[4], sm_mid[0], sm_mid[2], sm_fin[0]]).reshape(-1)
    dmodx = jnp.stack([sm_pre[1], sm_pre[3], sm_mid[5], sm_mid[1], sm_mid[3], sm_fin[1]]).reshape(-1)
    on = r["sm_post"][0].reshape(NH, DH)
    pieces = [dmodc, dmodx, sm_pre[4], sm_mid[7], sm_mid[6], sm_fin[2], on[:NH // 2].sum(0), on[NH // 2:].sum(0),
              r["d_lb"][:2].reshape(-1), _unlayout_wgk(r["d_wgk"]).reshape(-1), r["d_bgk"][0]]
    sizes = [p.shape[0] for p in pieces]
    pack = jnp.concatenate(pieces).reshape(-1, DH)
    pack_all, = _all_gather([pack], "ag_small_grads")
    tot = _sum_devices(pack_all).reshape(-1)
    offs = [sum(sizes[:i]) for i in range(len(sizes))]
    part = lambda i: tot[offs[i]:offs[i] + sizes[i]]
    dmodc_t, dmodx_t = part(0), part(1)
    g_b_mod = (dmodc_t + dmodx_t)[None]
    g_norms = [part(i)[None] for i in (2, 3, 4, 5)]
    g_hg_on, g_gla_on = part(6)[None], part(7)[None]
    lb0 = lax.dynamic_slice(part(8).reshape(2, HW), (0, me * (HW // N_DEV)), (2, HW // N_DEV))
    g_hg_lb = jnp.stack([lb0, -lb0])
    g_wgk = lax.dynamic_slice(part(9).reshape(2, 16, HW), (0, 0, me * (HW // N_DEV)), (2, 16, HW // N_DEV))[None]
    g_bgk = lax.dynamic_slice(part(10).reshape(2, HW), (0, me * (HW // N_DEV)), (2, HW // N_DEV))[None]

    dmx_all = pack_all.reshape(N_DEV, -1)[:, sizes[0]:sizes[0] + sizes[1]]
    d9 = jnp.concatenate([lax.dynamic_slice(dmodc_t[None], (0, me * n_mod), (1, n_mod)),
                          lax.dynamic_slice(dmx_all, (0, me * n_mod), (N_DEV, n_mod)),
                          jnp.zeros((16 - 1 - N_DEV, n_mod), F32)], axis=0)
    g_w_mod, dcc_part = _mod_bwd(a9, d9, w_mod[0])
    dcc_all, = _all_gather([dcc_part], "ag_c_ctx")
    g_c_ctx = _sum_devices(dcc_all)[0]

    shard = lambda d: jnp.transpose(d.reshape(d.shape[0], N_DEV, -1), (1, 0, 2)).astype(BF16)
    d_gu = r["d_w_gu"]
    send = [shard(_unlayout_w_in(r["d_w_in"])), shard(r["d_w_br_hg"]), shard(r["d_w_br_gla"]),
            r["d_w_out"].reshape(N_DEV, D // N_DEV, D).astype(BF16), shard(d_gu[:, :D_FF]), shard(d_gu[:, D_FF:]),
            r["d_w_down"].reshape(N_DEV, D_FF // N_DEV, D).astype(BF16)]
    recv = _exchange(send, "exchange_grads")
    moms = [(m_w_in, v_w_in), (m_w_br_hg, v_w_br_hg), (m_w_br_gla, v_w_br_gla), (m_w_out, v_w_out),
            (m_w_ff_gate, v_w_ff_gate), (m_w_ff_up, v_w_ff_up), (m_w_ff_down, v_w_ff_down)]
    names = ["w_in", "w_br_hg", "w_br_gla", "w_out", "w_ff_gate", "w_ff_up", "w_ff_down"]
    res = {}
    for nm, w, (m, v), g in zip(names, big, moms, recv):
        res[nm] = [o[None] for o in _adamw(g, w, m[0], v[0], "adamw_" + nm)]
    res["w_mod"] = [o[None] for o in _adamw(g_w_mod[None], w_mod[0], m_w_mod[0], v_w_mod[0], "adamw_w_mod")]

    small = [("c_ctx", c_ctx, m_c_ctx, v_c_ctx, g_c_ctx), ("b_mod", b_mod, m_b_mod, v_b_mod, g_b_mod),
             ("norm_pre1", norm_pre1, m_norm_pre1, v_norm_pre1, g_norms[0]),
             ("norm_post1", norm_post1, m_norm_post1, v_norm_post1, g_norms[1]),
             ("norm_pre2", norm_pre2, m_norm_pre2, v_norm_pre2, g_norms[2]),
             ("norm_post2", norm_post2, m_norm_post2, v_norm_post2, g_norms[3]),
             ("hg_lb", hg_lb, m_hg_lb, v_hg_lb, g_hg_lb), ("hg_onorm", hg_onorm, m_hg_onorm, v_hg_onorm, g_hg_on),
             ("gla_w_gk", gla_w_gk, m_gla_w_gk, v_gla_w_gk, g_wgk), ("gla_b_gk", gla_b_gk, m_gla_b_gk, v_gla_b_gk, g_bgk),
             ("gla_onorm", gla_onorm, m_gla_onorm, v_gla_onorm, g_gla_on)]
    flat = lambda k: jnp.concatenate([s[k].reshape(-1) for s in small]).reshape(-1, DH)
    outs = _adamw(flat(4)[None], flat(1), flat(2), flat(3), "adamw_small")
    off = 0
    for nm, w, _, _, _ in small:
        res[nm] = [o.reshape(-1)[off:off + w.size].reshape(w.shape) for o in outs]
        off += w.size

    order = ["c_ctx", "w_mod", "b_mod", "norm_pre1", "norm_post1", "norm_pre2", "norm_post2", "w_in", "hg_lb",
             "hg_onorm", "gla_w_gk", "gla_b_gk", "gla_onorm", "w_br_hg", "w_br_gla", "w_out", "w_ff_gate", "w_ff_up",
             "w_ff_down"]
    return (loss, grad_x, *[res[n][k] for k in range(4) for n in order])
```

```python
import functools

import jax
import jax.numpy as jnp
from jax import lax
from jax.experimental import pallas as pl
from jax.experimental.pallas import tpu as pltpu

F32 = jnp.float32
BF16 = jnp.bfloat16
HI = lax.Precision.HIGHEST

N_DEV = 8
D = 1024
CTX = 256
HW = 512
DH = 128
NH = 8
D_FF = 2816
EPS = 1e-6
GLA_NORM = 16.0
CHUNK = 64
TR = 256
NCT = CTX // TR
W_IN_COLS = 7168
MAIN0 = 2048
LR0 = 6656
LEVELS = (32, 16, 8)
EXP_CLAMP = 80.0
VMEM_LIMIT = 48 * 1024 * 1024

ADAM_LR, ADAM_B1, ADAM_B2, ADAM_EPS, ADAM_WD, ADAM_STEP = 0.001, 0.9, 0.999, 1e-08, 0.01, 10


def _cp(*sem):
    return pltpu.CompilerParams(dimension_semantics=sem, vmem_limit_bytes=VMEM_LIMIT)


def _sig(x):
    return jax.nn.sigmoid(x)


def _silu(x):
    return x * _sig(x)


def _dsilu(x):
    s = _sig(x)
    return s * (1.0 + x * (1.0 - s))


def _rstd(x):
    return lax.rsqrt(jnp.mean(x * x, axis=-1, keepdims=True) + EPS)


def _rms_bwd(a, y, r):
    return r * (a - y * (r * r) * jnp.mean(a * y, axis=-1, keepdims=True))


def _colsum(x):
    return jnp.sum(x, axis=0, keepdims=True)


def _dot(a, b, dims, precision=None):
    return lax.dot_general(a, b, (dims, ((), ())), preferred_element_type=F32, precision=precision)


NN = ((1,), (0,))
NT = ((1,), (1,))
TN = ((0,), (0,))

SCAN_HEADS_FWD = 4
SCAN_HEADS_BWD = 4


def _split_dot(m, x):
    mb = m.astype(BF16)
    x1 = x.astype(BF16)
    r1 = x - x1.astype(F32)
    x2 = r1.astype(BF16)
    x3 = (r1 - x2.astype(F32)).astype(BF16)
    return _dot(mb, x1, NN) + _dot(mb, x2, NN) + _dot(mb, x3, NN)


def _matmul(a, b, dims, out_dtype, name, tm, tn, tk, a_off=0, m_out=None):
    if dims == NN:
        m, k = a.shape[0], b.shape[0]
        n = b.shape[1]
        a_spec = pl.BlockSpec((tm, tk), lambda i, j, kk: (i, kk + a_off))
        b_spec = pl.BlockSpec((tk, tn), lambda i, j, kk: (kk, j))
    elif dims == NT:
        m, k = a.shape[0], b.shape[1]
        n = b.shape[0]
        a_spec = pl.BlockSpec((tm, tk), lambda i, j, kk: (i, kk + a_off))
        b_spec = pl.BlockSpec((tn, tk), lambda i, j, kk: (j, kk))
    else:
        m, k = (a.shape[1] if m_out is None else m_out), a.shape[0]
        n = b.shape[1]
        a_spec = pl.BlockSpec((tk, tm), lambda i, j, kk: (kk, i + a_off))
        b_spec = pl.BlockSpec((tk, tn), lambda i, j, kk: (kk, j))
    assert m % tm == 0 and n % tn == 0 and k % tk == 0, (name, m, n, k, tm, tn, tk)
    nk = k // tk

    def body(a_ref, b_ref, o_ref, *acc):
        part = _dot(a_ref[...], b_ref[...], dims)
        if nk == 1:
            o_ref[...] = part.astype(o_ref.dtype)
            return
        acc_ref, = acc
        kk = pl.program_id(2)

        @pl.when(kk == 0)
        def _():
            acc_ref[...] = part

        @pl.when(kk > 0)
        def _():
            acc_ref[...] += part

        @pl.when(kk == nk - 1)
        def _():
            o_ref[...] = acc_ref[...].astype(o_ref.dtype)

    return pl.pallas_call(
        body,
        name=name,
        grid=(m // tm, n // tn, nk),
        in_specs=[a_spec, b_spec],
        out_specs=pl.BlockSpec((tm, tn), lambda i, j, kk: (i, j)),
        out_shape=jax.ShapeDtypeStruct((m, n), out_dtype),
        scratch_shapes=[] if nk == 1 else [pltpu.VMEM((tm, tn), F32)],
        compiler_params=_cp("parallel", "parallel", "arbitrary"),
    )(a, b)


def _row(c):
    return pl.BlockSpec((TR, c), lambda i: (i, 0))


def _rowcol(width, cb):
    return pl.BlockSpec((TR, width), lambda i: (i, cb))


def _full(shape):
    return pl.BlockSpec(shape, lambda i: (0,) * len(shape))


def _mod_row(mc_ref, mx_ref, k, is_ctx):
    return jnp.where(is_ctx, mc_ref[k:k + 1, :], mx_ref[k:k + 1, :])


def _acc_row(ref, k, val):
    ref[k:k + 1, :] += val


def _acc_mod(ref, k, is_ctx, val):
    zero = jnp.zeros_like(val)
    ref[k:k + 1, :] += jnp.where(is_ctx, val, zero)
    ref[k + 1:k + 2, :] += jnp.where(is_ctx, zero, val)


def _prenorm(z, nw, modc, modx, i_shift, i_scale, name):
    t = z.shape[0]

    def body(z_ref, nw_ref, mc_ref, mx_ref, h_ref):
        is_ctx = pl.program_id(0) < NCT
        x = z_ref[...]
        n = x * _rstd(x) * nw_ref[...]
        h = n * (1.0 + _mod_row(mc_ref, mx_ref, i_scale, is_ctx)) + _mod_row(mc_ref, mx_ref, i_shift, is_ctx)
        h_ref[...] = h.astype(BF16)

    return pl.pallas_call(
        body, name=name, grid=(t // TR,),
        in_specs=[_row(D), _full((1, D)), _full((8, D)), _full((8, D))],
        out_specs=_row(D),
        out_shape=jax.ShapeDtypeStruct((t, D), BF16),
        compiler_params=_cp("parallel"),
    )(z, nw, modc, modx)


def _hg_lb(lb_ref, d):
    a0 = lb_ref[0, d:d + 1, :]
    a1 = lb_ref[1, d:d + 1, :]
    mx = jnp.maximum(a0, a1)
    e0 = jnp.exp(a0 - mx)
    e1 = jnp.exp(a1 - mx)
    return e0 / (e0 + e1)


def _log_sigmoid(x):
    return jnp.minimum(x, 0.0) - jnp.log(1.0 + jnp.exp(-jnp.abs(x)))


def _gates_fwd(p, hg_lb, wgk, bgk):
    t = p.shape[0]
    seg = lambda j: _rowcol(HW, MAIN0 // HW + j)

    def body(hq_ref, hi_ref, hf_ref, hb_ref, gq_ref, gk_ref, gv_ref, lr_ref, lb_ref, wgk_ref, bgk_ref,
             q_ref, v_ref, kf_ref, kb_ref, gf_ref, gb_ref):
        q_ref[:, :HW] = _silu(hq_ref[...])
        q_ref[:, HW:] = gq_ref[...] * (DH ** -0.5)
        v_ref[:, :HW] = hi_ref[...]
        v_ref[:, HW:] = gv_ref[...]
        xg = _dot(lr_ref[...].astype(BF16), wgk_ref[...], NN) + bgk_ref[...]
        for d, (raw_ref, k_ref, g_ref) in enumerate(((hf_ref, kf_ref, gf_ref), (hb_ref, kb_ref, gb_ref))):
            lbd = _hg_lb(lb_ref, d)
            f = lbd + (1.0 - lbd) * _sig(raw_ref[...])
            k_ref[:, :HW] = 1.0 - f
            k_ref[:, HW:] = gk_ref[...]
            g_ref[:, :HW] = jnp.log(f)
            g_ref[:, HW:] = _log_sigmoid(xg[:, d * HW:(d + 1) * HW]) * (1.0 / GLA_NORM)

    out = jax.ShapeDtypeStruct((t, D), F32)
    return pl.pallas_call(
        body, name="gates_fwd", grid=(t // TR,),
        in_specs=[seg(0), seg(1), seg(2), seg(3), seg(5), seg(6), seg(7), _rowcol(DH, LR0 // DH),
                  _full((2, 2, HW)), _full((DH, D)), _full((1, D))],
        out_specs=[_row(D)] * 6,
        out_shape=[out] * 6,
        compiler_params=_cp("parallel"),
    )(p, p, p, p, p, p, p, p, hg_lb, wgk, bgk)


def _post_fwd(o_fw, o_bw, p, onw):
    t = o_fw.shape[0]

    def body(of_ref, ob_ref, g1_ref, g2_ref, w_ref, y_ref):
        for h in range(NH):
            sl = slice(h * DH, (h + 1) * DH)
            o = of_ref[:, sl] + ob_ref[:, sl]
            g_ref = g1_ref if h < NH // 2 else g2_ref
            gs = slice((h % (NH // 2)) * DH, (h % (NH // 2) + 1) * DH)
            n = o * _rstd(o) * w_ref[:, sl]
            y_ref[:, sl] = (n * _silu(g_ref[:, gs])).astype(BF16)

    return pl.pallas_call(
        body, name="post_fwd", grid=(t // TR,),
        in_specs=[_row(D), _row(D), _rowcol(HW, MAIN0 // HW + 4), _rowcol(HW, MAIN0 // HW + 8), _full((1, D))],
        out_specs=_row(D),
        out_shape=jax.ShapeDtypeStruct((t, D), BF16),
        compiler_params=_cp("parallel"),
    )(o_fw, o_bw, p, p, onw)


def _merge_fwd(p, u1, u2):
    t = p.shape[0]

    def body(g1_ref, g2_ref, u1_ref, u2_ref, m_ref):
        m_ref[...] = (_sig(g1_ref[...]) * u1_ref[...] + _sig(g2_ref[...]) * u2_ref[...]).astype(BF16)

    return pl.pallas_call(
        body, name="merge_fwd", grid=(t // TR,),
        in_specs=[_rowcol(D, 0), _rowcol(D, 1), _row(D), _row(D)],
        out_specs=_row(D),
        out_shape=jax.ShapeDtypeStruct((t, D), BF16),
        compiler_params=_cp("parallel"),
    )(p, p, u1, u2)


def _mid_fwd(z, y1, nw_post, nw_pre, modc, modx):
    t = z.shape[0]

    def body(z_ref, y_ref, wpo_ref, wpr_ref, mc_ref, mx_ref, z1_ref, h_ref):
        is_ctx = pl.program_id(0) < NCT
        y = y_ref[...]
        z1 = z_ref[...] + _mod_row(mc_ref, mx_ref, 2, is_ctx) * (y * _rstd(y) * wpo_ref[...])
        z1_ref[...] = z1
        n = z1 * _rstd(z1) * wpr_ref[...]
        h = n * (1.0 + _mod_row(mc_ref, mx_ref, 4, is_ctx)) + _mod_row(mc_ref, mx_ref, 3, is_ctx)
        h_ref[...] = h.astype(BF16)

    return pl.pallas_call(
        body, name="mid_fwd", grid=(t // TR,),
        in_specs=[_row(D), _row(D), _full((1, D)), _full((1, D)), _full((8, D)), _full((8, D))],
        out_specs=[_row(D), _row(D)],
        out_shape=[jax.ShapeDtypeStruct((t, D), F32), jax.ShapeDtypeStruct((t, D), BF16)],
        compiler_params=_cp("parallel"),
    )(z, y1, nw_post, nw_pre, modc, modx)


def _swiglu_fwd(uv):
    t = uv.shape[0]

    def body(u_ref, v_ref, a_ref):
        a_ref[...] = (_silu(u_ref[...]) * v_ref[...]).astype(BF16)

    return pl.pallas_call(
        body, name="swiglu_fwd", grid=(t // TR,),
        in_specs=[_rowcol(D_FF, 0), _rowcol(D_FF, 1)],
        out_specs=_row(D_FF),
        out_shape=jax.ShapeDtypeStruct((t, D_FF), BF16),
        compiler_params=_cp("parallel"),
    )(uv, uv)


def _swiglu_bwd(uv, da):
    t = uv.shape[0]

    def body(u_ref, v_ref, da_ref, d_ref):
        u = u_ref[...]
        d = da_ref[...]
        d_ref[:, :D_FF] = (d * v_ref[...] * _dsilu(u)).astype(BF16)
        d_ref[:, D_FF:] = (d * _silu(u)).astype(BF16)

    return pl.pallas_call(
        body, name="swiglu_bwd", grid=(t // TR,),
        in_specs=[_rowcol(D_FF, 0), _rowcol(D_FF, 1), _row(D_FF)],
        out_specs=_row(2 * D_FF),
        out_shape=jax.ShapeDtypeStruct((t, 2 * D_FF), BF16),
        compiler_params=_cp("parallel"),
    )(uv, uv, da)


def _final(z1, y2, target, nw, modc, modx):
    t = z1.shape[0]

    def body(z1_ref, y_ref, tg_ref, w_ref, mc_ref, mx_ref, dz_ref, dy_ref, loss_ref, sm_ref):
        i = pl.program_id(0)
        is_ctx = i < NCT

        @pl.when(i == 0)
        def _():
            loss_ref[...] = jnp.zeros_like(loss_ref)
            sm_ref[...] = jnp.zeros_like(sm_ref)

        g = _mod_row(mc_ref, mx_ref, 5, is_ctx)
        y = y_ref[...]
        r = _rstd(y)
        w = w_ref[...]
        yr = y * r
        n = yr * w
        e = z1_ref[...] + g * n - tg_ref[...]
        lat = jnp.where(is_ctx, 0.0, 1.0)
        loss_ref[...] += lat * _colsum(e * e)
        dz = e * (lat / D)
        dz_ref[...] = dz
        _acc_mod(sm_ref, 0, is_ctx, _colsum(dz * n))
        dn = dz * g
        _acc_row(sm_ref, 2, _colsum(dn * yr))
        dy_ref[...] = _rms_bwd(dn * w, y, r).astype(BF16)

    return pl.pallas_call(
        body, name="final", grid=(t // TR,),
        in_specs=[_row(D), _row(D), pl.BlockSpec((TR, D), lambda i: (jnp.maximum(i - NCT, 0), 0)),
                  _full((1, D)), _full((8, D)), _full((8, D))],
        out_specs=[_row(D), _row(D), _full((1, D)), _full((8, D))],
        out_shape=[jax.ShapeDtypeStruct((t, D), F32), jax.ShapeDtypeStruct((t, D), BF16),
                   jax.ShapeDtypeStruct((1, D), F32), jax.ShapeDtypeStruct((8, D), F32)],
        compiler_params=_cp("arbitrary"),
    )(z1, y2, target, nw, modc, modx)


def _mid_bwd(dh2, dz, z, z1, y1, nw_post, nw_pre, modc, modx):
    t = z.shape[0]

    def body(dh_ref, dz_ref, z_ref, z1_ref, y_ref, wpo_ref, wpr_ref, mc_ref, mx_ref, dzo_ref, dy_ref, sm_ref):
        i = pl.program_id(0)
        is_ctx = i < NCT

        @pl.when(i == 0)
        def _():
            sm_ref[...] = jnp.zeros_like(sm_ref)

        dh = dh_ref[...]
        z1 = z1_ref[...]
        r = _rstd(z1)
        zr = z1 * r
        wpr = wpr_ref[...]
        n = zr * wpr
        _acc_mod(sm_ref, 0, is_ctx, _colsum(dh))
        _acc_mod(sm_ref, 2, is_ctx, _colsum(dh * n))
        dn = dh * (1.0 + _mod_row(mc_ref, mx_ref, 4, is_ctx))
        _acc_row(sm_ref, 6, _colsum(dn * zr))
        dz1 = dz_ref[...] + _rms_bwd(dn * wpr, z1, r)
        dzo_ref[...] = dz1
        y = y_ref[...]
        r1 = _rstd(y)
        yr = y * r1
        wpo = wpo_ref[...]
        g = _mod_row(mc_ref, mx_ref, 2, is_ctx)
        _acc_mod(sm_ref, 4, is_ctx, _colsum(dz1 * (yr * wpo)))
        dn1 = dz1 * g
        _acc_row(sm_ref, 7, _colsum(dn1 * yr))
        dy_ref[...] = _rms_bwd(dn1 * wpo, y, r1).astype(BF16)

    return pl.pallas_call(
        body, name="mid_bwd", grid=(t // TR,),
        in_specs=[_row(D)] * 5 + [_full((1, D)), _full((1, D)), _full((8, D)), _full((8, D))],
        out_specs=[_row(D), _row(D), _full((8, D))],
        out_shape=[jax.ShapeDtypeStruct((t, D), F32), jax.ShapeDtypeStruct((t, D), BF16),
                   jax.ShapeDtypeStruct((8, D), F32)],
        compiler_params=_cp("arbitrary"),
    )(dh2, dz, z, z1, y1, nw_post, nw_pre, modc, modx)


def _pre_bwd(dh1, dz, z, nw, modc, modx):
    t = z.shape[0]

    def body(dh_ref, dz_ref, z_ref, w_ref, mc_ref, mx_ref, dzo_ref, sm_ref):
        i = pl.program_id(0)
        is_ctx = i < NCT

        @pl.when(i == 0)
        def _():
            sm_ref[...] = jnp.zeros_like(sm_ref)

        dh = dh_ref[...]
        x = z_ref[...]
        r = _rstd(x)
        xr = x * r
        w = w_ref[...]
        _acc_mod(sm_ref, 0, is_ctx, _colsum(dh))
        _acc_mod(sm_ref, 2, is_ctx, _colsum(dh * (xr * w)))
        dn = dh * (1.0 + _mod_row(mc_ref, mx_ref, 1, is_ctx))
        _acc_row(sm_ref, 4, _colsum(dn * xr))
        dzo_ref[...] = dz_ref[...] + _rms_bwd(dn * w, x, r)

    return pl.pallas_call(
        body, name="pre_bwd", grid=(t // TR,),
        in_specs=[_row(D)] * 3 + [_full((1, D)), _full((8, D)), _full((8, D))],
        out_specs=[_row(D), _full((8, D))],
        out_shape=[jax.ShapeDtypeStruct((t, D), F32), jax.ShapeDtypeStruct((8, D), F32)],
        compiler_params=_cp("arbitrary"),
    )(dh1, dz, z, nw, modc, modx)


def _merge_bwd(dm, p, u1, u2):
    t = dm.shape[0]

    def body(dm_ref, g1_ref, g2_ref, u1_ref, u2_ref, du1_ref, du2_ref, dg_ref):
        dm_ = dm_ref[...]
        s1 = _sig(g1_ref[...])
        s2 = _sig(g2_ref[...])
        du1_ref[...] = (dm_ * s1).astype(BF16)
        du2_ref[...] = (dm_ * s2).astype(BF16)
        dg_ref[:, :D] = (dm_ * u1_ref[...] * s1 * (1.0 - s1)).astype(BF16)
        dg_ref[:, D:] = (dm_ * u2_ref[...] * s2 * (1.0 - s2)).astype(BF16)

    return pl.pallas_call(
        body, name="merge_bwd", grid=(t // TR,),
        in_specs=[_row(D), _rowcol(D, 0), _rowcol(D, 1), _row(D), _row(D)],
        out_specs=[_row(D), _row(D), _row(2 * D)],
        out_shape=[jax.ShapeDtypeStruct((t, D), BF16), jax.ShapeDtypeStruct((t, D), BF16),
                   jax.ShapeDtypeStruct((t, 2 * D), BF16)],
        compiler_params=_cp("parallel"),
    )(dm, p, p, u1, u2)


def _post_bwd(dy_hg, dy_gla, o_fw, o_bw, p, onw):
    t = o_fw.shape[0]

    def body(d1_ref, d2_ref, of_ref, ob_ref, g1_ref, g2_ref, w_ref, do_ref, dg_ref, sm_ref):
        @pl.when(pl.program_id(0) == 0)
        def _():
            sm_ref[...] = jnp.zeros_like(sm_ref)

        for h in range(NH):
            sl = slice(h * DH, (h + 1) * DH)
            gs = slice((h % (NH // 2)) * DH, (h % (NH // 2) + 1) * DH)
            g_ref, d_ref = (g1_ref, d1_ref) if h < NH // 2 else (g2_ref, d2_ref)
            o = of_ref[:, sl] + ob_ref[:, sl]
            r = _rstd(o)
            orr = o * r
            w = w_ref[:, sl]
            gt = g_ref[:, gs]
            dy = d_ref[:, gs]
            dg_ref[:, sl] = (dy * (orr * w) * _dsilu(gt)).astype(BF16)
            dn = dy * _silu(gt)
            sm_ref[0:1, sl] += _colsum(dn * orr)
            do_ref[:, sl] = _rms_bwd(dn * w, o, r)

    return pl.pallas_call(
        body, name="post_bwd", grid=(t // TR,),
        in_specs=[_row(HW), _row(HW), _row(D), _row(D), _rowcol(HW, MAIN0 // HW + 4), _rowcol(HW, MAIN0 // HW + 8),
                  _full((1, D))],
        out_specs=[_row(D), _row(D), _full((8, D))],
        out_shape=[jax.ShapeDtypeStruct((t, D), F32), jax.ShapeDtypeStruct((t, D), BF16),
                   jax.ShapeDtypeStruct((8, D), F32)],
        compiler_params=_cp("arbitrary"),
    )(dy_hg, dy_gla, o_fw, o_bw, p, p, onw)


def _gates_bwd(p, hg_lb, wgk, bgk, dgm, dgo, dq_f, dq_b, dv_f, dv_b, dk_f, dk_b, dg_f, dg_b):
    t = p.shape[0]
    seg = lambda j: _rowcol(HW, MAIN0 // HW + j)

    def body(hq_ref, hf_ref, hb_ref, lr_ref, lb_ref, wgk_ref, bgk_ref, dgm_ref, dgo_ref,
             dqf_ref, dqb_ref, dvf_ref, dvb_ref, dkf_ref, dkb_ref, dgf_ref, dgb_ref,
             dp_ref, dlb_ref, dw_ref, db_ref):
        @pl.when(pl.program_id(0) == 0)
        def _():
            dlb_ref[...] = jnp.zeros_like(dlb_ref)
            dw_ref[...] = jnp.zeros_like(dw_ref)
            db_ref[...] = jnp.zeros_like(db_ref)

        c0 = MAIN0

        def put(j, val):
            dp_ref[:, c0 + j * HW:c0 + (j + 1) * HW] = val.astype(BF16)

        dp_ref[:, :MAIN0] = dgm_ref[...]
        dq = dqf_ref[...] + dqb_ref[...]
        dv = dvf_ref[...] + dvb_ref[...]
        put(0, dq[:, :HW] * _dsilu(hq_ref[...]))
        put(1, dv[:, :HW])
        put(5, dq[:, HW:] * (DH ** -0.5))
        put(7, dv[:, HW:])
        put(6, dkf_ref[:, HW:] + dkb_ref[:, HW:])
        dp_ref[:, c0 + 4 * HW:c0 + 5 * HW] = dgo_ref[:, :HW]
        dp_ref[:, c0 + 8 * HW:c0 + 9 * HW] = dgo_ref[:, HW:]
        lr = lr_ref[...].astype(BF16)
        xg = _dot(lr, wgk_ref[...], NN) + bgk_ref[...]
        dxg = []
        for d, (raw_ref, dk_ref, dg_ref) in enumerate(((hf_ref, dkf_ref, dgf_ref), (hb_ref, dkb_ref, dgb_ref))):
            lbd = _hg_lb(lb_ref, d)
            s = _sig(raw_ref[...])
            f = lbd + (1.0 - lbd) * s
            df = dg_ref[:, :HW] / f - dk_ref[:, :HW]
            put(2 + d, df * (1.0 - lbd) * s * (1.0 - s))
            dlb_ref[d:d + 1, :] += _colsum(df * (1.0 - s)) * (lbd * (1.0 - lbd))
            dxg.append(dg_ref[:, HW:] * (1.0 / GLA_NORM) * _sig(-xg[:, d * HW:(d + 1) * HW]))
        dxg = jnp.concatenate(dxg, axis=1)
        db_ref[0:1, :] += _colsum(dxg)
        dxg_b = dxg.astype(BF16)
        dw_ref[...] += _dot(lr, dxg_b, TN)
        dp_ref[:, LR0:LR0 + DH] = _dot(dxg_b, wgk_ref[...], NT).astype(BF16)
        dp_ref[:, LR0 + DH:] = jnp.zeros((TR, W_IN_COLS - LR0 - DH), BF16)

    return pl.pallas_call(
        body, name="gates_bwd", grid=(t // TR,),
        in_specs=[seg(0), seg(2), seg(3), _rowcol(DH, LR0 // DH), _full((2, 2, HW)), _full((DH, D)), _full((1, D)),
                  _row(2 * D), _row(D)] + [_row(D)] * 8,
        out_specs=[_row(W_IN_COLS), _full((8, HW)), _full((DH, D)), _full((8, D))],
        out_shape=[jax.ShapeDtypeStruct((t, W_IN_COLS), BF16), jax.ShapeDtypeStruct((8, HW), F32),
                   jax.ShapeDtypeStruct((DH, D), F32), jax.ShapeDtypeStruct((8, D), F32)],
        compiler_params=_cp("arbitrary"),
    )(p, p, p, p, hg_lb, wgk, bgk, dgm, dgo, dq_f, dq_b, dv_f, dv_b, dk_f, dk_b, dg_f, dg_b)


def _scan_consts(rev):
    r = lax.broadcasted_iota(jnp.int32, (CHUNK, CHUNK), 0)
    u = lax.broadcasted_iota(jnp.int32, (CHUNK, CHUNK), 1)
    rp = lax.broadcasted_iota(jnp.int32, (CHUNK, 1), 0)
    if rev:
        r, u, rp = CHUNK - 1 - r, CHUNK - 1 - u, CHUNK - 1 - rp
    tri = jnp.where(u <= r, 1.0, 0.0).astype(F32)
    tri_t = jnp.where(r <= u, 1.0, 0.0).astype(F32)
    lv = []
    for b in LEVELS:
        sh = b.bit_length() - 1
        pair = ((r >> sh) == (u >> sh) + 1) & (((u >> sh) & 1) == 0)
        pair_t = ((u >> sh) == (r >> sh) + 1) & (((r >> sh) & 1) == 0)
        tside = ((rp >> sh) & 1) == 1
        lv.append((pair, pair_t, tside))
    bd = LEVELS[-1].bit_length() - 1
    diag = ((r >> bd) == (u >> bd)) & (u <= r)
    diag_t = ((r >> bd) == (u >> bd)) & (r <= u)
    return tri, tri_t, lv, diag, diag_t


def _row_of(pos, rev):
    return CHUNK - 1 - pos if rev else pos


def _chunk_terms(cum, b_scr, consts, rev):
    _, _, lv, _, _ = consts
    terms = []
    for b, (_, _, tside) in zip(LEVELS, lv):
        pieces = []
        for j in range(CHUNK // (2 * b)):
            row = _row_of(2 * b * j + b - 1, rev)
            pieces.append(jnp.broadcast_to(b_scr[row:row + 1, :], (2 * b, DH)))
        if rev:
            pieces = pieces[::-1]
        bnd = pieces[0] if len(pieces) == 1 else jnp.concatenate(pieces, axis=0)
        w = jnp.exp(jnp.minimum(jnp.where(tside, cum - bnd, bnd - cum), 0.0))
        wq = jnp.where(tside, w, 0.0)
        wk = jnp.where(tside, 0.0, w)
        terms.append((wq, wk))
    b = LEVELS[-1]
    pieces = []
    for j in range(CHUNK // b):
        if j == 0:
            pieces.append(jnp.zeros((b, DH), F32))
        else:
            row = _row_of(b * j - 1, rev)
            pieces.append(jnp.broadcast_to(b_scr[row:row + 1, :], (b, DH)))
    if rev:
        pieces = pieces[::-1]
    start = jnp.concatenate(pieces, axis=0)
    wq = jnp.exp(jnp.minimum(cum - start, 0.0))
    wk = jnp.exp(jnp.minimum(start - cum, EXP_CLAMP))
    terms.append((wq, wk))
    return terms


def _run_staged(units):
    live = list(units)
    while live:
        nxt = []
        for u in live:
            try:
                next(u)
                nxt.append(u)
            except StopIteration:
                pass
        live = nxt


SCAN_TB = 256
SCAN_CB = SCAN_TB // CHUNK


def _block_order(i, ntb, rev):
    nctx = CTX // SCAN_TB
    if not rev:
        return i
    return jnp.where(i < nctx, nctx - 1 - i, ntb - 1 - (i - nctx))


def _chunk_in_block(j, rev):
    return SCAN_CB - 1 - j if rev else j


def _scan_fwd(q, k, v, g, rev):
    t = q.shape[0]
    nc = t // CHUNK
    hpb = SCAN_HEADS_FWD

    def body(q_ref, k_ref, v_ref, g_ref, o_ref, st_ref, s_scr, b_scr):
        consts = _scan_consts(rev)
        _, _, lv, diag, _ = consts
        masks = [pair for pair, _, _ in lv] + [diag]

        @pl.when(pl.program_id(1) == 0)
        def _():
            s_scr[...] = jnp.zeros_like(s_scr)

        tri = consts[0]
        state = {hh: s_scr[hh] for hh in range(hpb)}

        def unit(hh, j):
            sl = slice(hh * DH, (hh + 1) * DH)
            c = _chunk_in_block(j, rev)
            rows = slice(c * CHUNK, (c + 1) * CHUNK)
            b_ref = b_scr.at[hh * SCAN_CB + j]
            qc, kc, vc, gc = q_ref[rows, sl], k_ref[rows, sl], v_ref[rows, sl], g_ref[rows, sl]
            cum = _split_dot(tri, gc)
            b_ref[...] = cum
            yield
            terms = _chunk_terms(cum, b_ref, consts, rev)
            ops = [((qc * wq).astype(BF16), (kc * wk).astype(BF16)) for wq, wk in terms]
            tot = _colsum(gc)
            qe = (qc * jnp.exp(cum)).astype(BF16)
            ke = (kc * jnp.exp(tot - cum)).astype(BF16)
            vb = vc.astype(BF16)
            yield
            scs = [_dot(qt, kt, NT) for qt, kt in ops]
            kv = _dot(vb, ke, TN)
            yield
            a = jnp.zeros((CHUNK, CHUNK), F32)
            for sc, m in zip(scs, masks):
                a = a + jnp.where(m, sc, 0.0)
            o_intra = _dot(a.astype(BF16), vb, NN)
            yield
            st = state[hh]
            st_ref[hh, c] = st
            o_ref[rows, sl] = o_intra + _dot(qe, st.astype(BF16), NT)
            state[hh] = st * jnp.exp(tot) + kv
            yield

        _run_staged([unit(hh, j) for hh in range(hpb) for j in range(SCAN_CB)])
        for hh in range(hpb):
            s_scr[hh] = state[hh]

    ntb = t // SCAN_TB
    col = pl.BlockSpec((SCAN_TB, hpb * DH), lambda h, i: (_block_order(i, ntb, rev), h))
    return pl.pallas_call(
        body, name="scan_fwd_" + ("bw" if rev else "fw"), grid=(NH // hpb, ntb),
        in_specs=[col] * 4,
        out_specs=[col, pl.BlockSpec((hpb, SCAN_CB, DH, DH), lambda h, i: (h, _block_order(i, ntb, rev), 0, 0))],
        out_shape=[jax.ShapeDtypeStruct((t, D), F32), jax.ShapeDtypeStruct((NH, nc, DH, DH), F32)],
        scratch_shapes=[pltpu.VMEM((hpb, DH, DH), F32), pltpu.VMEM((hpb * SCAN_CB, CHUNK, DH), F32)],
        compiler_params=_cp("parallel", "arbitrary"),
    )(q, k, v, g)


def _scan_bwd(q, k, v, g, do, states, rev):
    t = q.shape[0]
    nc = t // CHUNK
    hpb = SCAN_HEADS_BWD

    def body(q_ref, k_ref, v_ref, g_ref, do_ref, st_ref, dq_ref, dk_ref, dv_ref, dg_ref, ds_scr, b_scr):
        consts = _scan_consts(rev)
        _, tri_t, lv, diag, diag_t = consts
        masks = [(pair, pair_t) for pair, pair_t, _ in lv] + [(diag, diag_t)]
        @pl.when(pl.program_id(1) == 0)
        def _():
            ds_scr[...] = jnp.zeros_like(ds_scr)

        tri = consts[0]
        dstate = {hh: ds_scr[hh] for hh in range(hpb)}

        def unit(hh, jj):
            sl = slice(hh * DH, (hh + 1) * DH)
            c = _chunk_in_block(SCAN_CB - 1 - jj, rev)
            rows = slice(c * CHUNK, (c + 1) * CHUNK)
            b_ref = b_scr.at[hh * SCAN_CB + jj]
            qc, kc, vc, gc = q_ref[rows, sl], k_ref[rows, sl], v_ref[rows, sl], g_ref[rows, sl]
            dob = do_ref[rows, sl].astype(BF16)
            vb = vc.astype(BF16)
            cum = _split_dot(tri, gc)
            b_ref[...] = cum
            da = _dot(dob, vb, NT)
            da_t = _dot(vb, dob, NT)
            yield
            terms = _chunk_terms(cum, b_ref, consts, rev)
            ops = [((qc * wq).astype(BF16), (kc * wk).astype(BF16)) for wq, wk in terms]
            tot = _colsum(gc)
            e_tot = jnp.exp(tot)
            e_b = jnp.exp(cum)
            e_t = jnp.exp(tot - cum)
            qeb = (qc * e_b).astype(BF16)
            keb = (kc * e_t).astype(BF16)
            dal = [(jnp.where(m, da, 0.0).astype(BF16), jnp.where(m_t, da_t, 0.0).astype(BF16)) for m, m_t in masks]
            yield
            ats = [_dot(ktb, qtb, NT) for qtb, ktb in ops]
            dqts = [_dot(d, ktb, NN) for (d, _), (_, ktb) in zip(dal, ops)]
            dkts = [_dot(d_t, qtb, NN) for (_, d_t), (qtb, _) in zip(dal, ops)]
            qd = _dot(dob, qeb, TN)
            yield
            a_t = jnp.zeros((CHUNK, CHUNK), F32)
            dq = jnp.zeros((CHUNK, DH), F32)
            dk = jnp.zeros((CHUNK, DH), F32)
            db = jnp.zeros((CHUNK, DH), F32)
            for at, dqt, dkt, (wq, wk), (qtb, ktb), (_, m_t) in zip(ats, dqts, dkts, terms, ops, masks):
                a_t = a_t + jnp.where(m_t, at, 0.0)
                dq = dq + dqt * wq
                dk = dk + dkt * wk
                db = db + dqt * qtb.astype(F32) - dkt * ktb.astype(F32)
            dv_intra = _dot(a_t.astype(BF16), dob, NN)
            st = st_ref[hh, c]
            stb = st.astype(BF16)
            dqe = _dot(dob, stb, NN)
            yield
            dst = dstate[hh]
            dstb = dst.astype(BF16)
            dstate[hh] = dst * e_tot + qd
            dv_ref[rows, sl] = dv_intra + _dot(keb, dstb, NT)
            dke = _dot(vb, dstb, NN)
            yield
            qe = qeb.astype(F32)
            ke = keb.astype(F32)
            dq_ref[rows, sl] = dq + dqe * e_b
            dk_ref[rows, sl] = dk + dke * e_t
            db = db + dqe * qe - dke * ke
            dtot = _colsum(dstb.astype(F32) * stb.astype(F32)) * e_tot + _colsum(dke * ke)
            dg_ref[rows, sl] = _split_dot(tri_t, db) + dtot
            yield

        _run_staged([unit(hh, jj) for hh in range(hpb) for jj in range(SCAN_CB)])
        for hh in range(hpb):
            ds_scr[hh] = dstate[hh]

    ntb = t // SCAN_TB
    blk = lambda i: _block_order(ntb - 1 - i, ntb, rev)
    col = pl.BlockSpec((SCAN_TB, hpb * DH), lambda h, i: (blk(i), h))
    out = jax.ShapeDtypeStruct((t, D), F32)
    return pl.pallas_call(
        body, name="scan_bwd_" + ("bw" if rev else "fw"), grid=(NH // hpb, ntb),
        in_specs=[col] * 5 + [pl.BlockSpec((hpb, SCAN_CB, DH, DH), lambda h, i: (h, blk(i), 0, 0))],
        out_specs=[col] * 4,
        out_shape=[out] * 4,
        scratch_shapes=[pltpu.VMEM((hpb, DH, DH), F32), pltpu.VMEM((hpb * SCAN_CB, CHUNK, DH), F32)],
        compiler_params=_cp("parallel", "arbitrary"),
    )(q, k, v, g, do, states)


W_IN_REF = 6688
GATE0 = 4640
LRW = 32


def _layout_w_in(w):
    return jnp.concatenate([w[:, GATE0:], w[:, :GATE0 - LRW], w[:, GATE0 - LRW:GATE0],
                            jnp.zeros((w.shape[0], W_IN_COLS - W_IN_REF), w.dtype)], axis=1)


def _unlayout_w_in(d):
    return jnp.concatenate([d[:, MAIN0:LR0 + LRW], d[:, :MAIN0]], axis=1)


def _layout_wgk(w):
    r = w.shape[1]
    top = jnp.concatenate([w[0], jnp.zeros_like(w[0])], axis=1)
    bot = jnp.concatenate([jnp.zeros_like(w[1]), w[1]], axis=1)
    return jnp.concatenate([top, bot, jnp.zeros((DH - 2 * r, D), w.dtype)], axis=0)


def _unlayout_wgk(d, r=16):
    return jnp.stack([d[:r, :HW], d[r:2 * r, HW:]])


def _local_step(z, target, modc, modx, norms, onw, hg_lb, wgk, bgk, w_in, get_mix, get_ffn, send):
    n_pre1, n_post1, n_pre2, n_post2 = norms
    t = z.shape[0]
    tm = 768 if t % 768 == 0 else 256
    h1 = _prenorm(z, n_pre1, modc, modx, 0, 1, "prenorm1")
    p = _matmul(h1, w_in, NN, F32, "mm_in", tm, 512, D)
    q, v, k_f, k_b, g_f, g_b = _gates_fwd(p, hg_lb, wgk, bgk)
    o_f, st_f = _scan_fwd(q, k_f, v, g_f, False)
    o_b, st_b = _scan_fwd(q, k_b, v, g_b, True)
    y = _post_fwd(o_f, o_b, p, onw)
    w_br_hg, w_br_gla, w_out = get_mix(y)
    u1 = _matmul(y, w_br_hg, NN, F32, "mm_br_hg", tm, 512, HW, a_off=0)
    u2 = _matmul(y, w_br_gla, NN, F32, "mm_br_gla", tm, 512, HW, a_off=1)
    merged = _merge_fwd(p, u1, u2)
    y1 = _matmul(merged, w_out, NN, F32, "mm_out", tm, 512, D)
    z1, h2 = _mid_fwd(z, y1, n_post1, n_pre2, modc, modx)
    w_gu, w_down = get_ffn(h2)
    uv = _matmul(h2, w_gu, NN, F32, "mm_gu", tm, 512, D)
    act = _swiglu_fwd(uv)
    y2 = _matmul(act, w_down, NN, F32, "mm_down", tm, 512, D_FF // 2)
    dz, dy2, loss_vec, sm_final = _final(z1, y2, target, n_post2, modc, modx)
    dact = _matmul(dy2, w_down, NT, F32, "mm_down_dx", tm, D_FF // 2, D)
    d_w_down = _matmul(act, dy2, TN, F32, "mm_down_dw", D_FF // 2, 512, t)
    duv = _swiglu_bwd(uv, dact)
    dh2 = _matmul(duv, w_gu, NT, F32, "mm_gu_dx", tm, 512, D_FF // 2)
    d_w_gu = _matmul(h2, duv, TN, F32, "mm_gu_dw", 512, 512, t)
    dh2 = send(("w_down", "w_gu"), (d_w_down, d_w_gu), dh2)
    dz, dy1, sm_mid = _mid_bwd(dh2, dz, z, z1, y1, n_post1, n_pre2, modc, modx)
    dmerged = _matmul(dy1, w_out, NT, F32, "mm_out_dx", tm, 512, D)
    d_w_out = _matmul(merged, dy1, TN, F32, "mm_out_dw", 512, 512, t)
    du1, du2, dgm = _merge_bwd(dmerged, p, u1, u2)
    dy_hg = _matmul(du1, w_br_hg, NT, F32, "mm_br_hg_dx", tm, HW, D)
    dy_gla = _matmul(du2, w_br_gla, NT, F32, "mm_br_gla_dx", tm, HW, D)
    d_w_br_hg = _matmul(y, du1, TN, F32, "mm_br_hg_dw", HW, 512, t, a_off=0, m_out=HW)
    d_w_br_gla = _matmul(y, du2, TN, F32, "mm_br_gla_dw", HW, 512, t, a_off=1, m_out=HW)
    dy_hg = send(("w_out", "w_br_hg", "w_br_gla"), (d_w_out, d_w_br_hg, d_w_br_gla), dy_hg)
    do, dgo, sm_post = _post_bwd(dy_hg, dy_gla, o_f, o_b, p, onw)
    dq_f, dk_f, dv_f, dg_f = _scan_bwd(q, k_f, v, g_f, do, st_f, False)
    dq_b, dk_b, dv_b, dg_b = _scan_bwd(q, k_b, v, g_b, do, st_b, True)
    dp, d_lb, d_wgk, d_bgk = _gates_bwd(p, hg_lb, wgk, bgk, dgm, dgo, dq_f, dq_b, dv_f, dv_b, dk_f, dk_b, dg_f, dg_b)
    d_w_in = _matmul(h1, dp, TN, F32, "mm_in_dw", 512, 512, t)
    dp = send(("w_in",), (d_w_in,), dp)
    dh1 = _matmul(dp, w_in, NT, F32, "mm_in_dx", tm, 512, 1024)
    dz, sm_pre = _pre_bwd(dh1, dz, z, n_pre1, modc, modx)
    return dict(loss_vec=loss_vec, dz=dz, sm_final=sm_final, sm_mid=sm_mid, sm_post=sm_post, sm_pre=sm_pre,
                d_lb=d_lb, d_wgk=d_wgk, d_bgk=d_bgk)


MESH = pl.DeviceIdType.MESH
ANY = pl.BlockSpec(memory_space=pl.ANY)
N_REL = N_DEV - 1


def _place():
    return lax.axis_index("x"), lax.axis_index("y"), lax.axis_index("c")


def _slot(p):
    return 4 * p[0] + 2 * p[1] + p[2]


def _all_gather(arrays, name):
    n = len(arrays)

    def body(*refs):
        ins, outs = refs[:n], refs[n:2 * n]
        send_sems, recv_sems, local_sems = refs[2 * n:]
        x, y, c = _place()
        me, sibling = (x, y, c), (x, y, 1 - c)
        chips = [(1 - x, y), (x, 1 - y), (1 - x, 1 - y)]

        def copy(a, k, block, to, src=None):
            dst = outs[a].at[_slot(block)]
            return pltpu.make_async_remote_copy(
                src_ref=dst if src is None else src, dst_ref=dst,
                send_sem=send_sems.at[N_REL * a + k], recv_sem=recv_sems.at[N_REL * a + k],
                device_id=to, device_id_type=MESH)

        mine = [pltpu.make_async_copy(ins[a], outs[a].at[_slot(me)], local_sems.at[a]) for a in range(n)]
        for cp in mine:
            cp.start()
        first = []
        for a in range(n):
            first.append(copy(a, 0, me, sibling, src=ins[a]))
            first += [copy(a, 1 + j, me, (*chip, c), src=ins[a]) for j, chip in enumerate(chips)]
        for cp in first:
            cp.start()
        passed = []
        for j, chip in enumerate(chips):
            for a in range(n):
                copy(a, 1 + j, (*chip, c), me).wait_recv()
                fwd = copy(a, 4 + j, (*chip, c), sibling)
                fwd.start()
                passed.append(fwd)
        for a in range(n):
            copy(a, 0, sibling, me).wait_recv()
        for j, chip in enumerate(chips):
            for a in range(n):
                copy(a, 4 + j, (*chip, 1 - c), me).wait_recv()
        for cp in first + passed:
            cp.wait_send()
        for cp in mine:
            cp.wait()

    return pl.pallas_call(
        body, name=name,
        in_specs=[ANY] * n, out_specs=[ANY] * n,
        out_shape=[jax.ShapeDtypeStruct((N_DEV,) + a.shape, a.dtype) for a in arrays],
        scratch_shapes=[pltpu.SemaphoreType.DMA((N_REL * n,)), pltpu.SemaphoreType.DMA((N_REL * n,)),
                        pltpu.SemaphoreType.DMA((n,))],
    )(*arrays)


def _exchange(arrays, name):
    n = len(arrays)

    def body(*refs):
        ins, outs = refs[:n], refs[n:2 * n]
        send_sems, recv_sems, local_sems = refs[2 * n:]
        x, y, c = _place()
        me = _slot((x, y, c))
        mine = [pltpu.make_async_copy(ins[a].at[me], outs[a].at[me], local_sems.at[a]) for a in range(n)]
        for cp in mine:
            cp.start()
        copies = []
        for a in range(n):
            for k in range(1, N_DEV):
                flip = lambda v, bit: 1 - v if bit else v
                peer = (flip(x, k & 4), flip(y, k & 2), flip(c, k & 1))
                copies.append(pltpu.make_async_remote_copy(
                    src_ref=ins[a].at[_slot(peer)], dst_ref=outs[a].at[me],
                    send_sem=send_sems.at[N_REL * a + k - 1], recv_sem=recv_sems.at[N_REL * a + k - 1],
                    device_id=peer, device_id_type=MESH))
                copies[-1].start()
        i = 0
        for a in range(n):
            for k in range(1, N_DEV):
                flip = lambda v, bit: 1 - v if bit else v
                peer = (flip(x, k & 4), flip(y, k & 2), flip(c, k & 1))
                pltpu.make_async_remote_copy(
                    src_ref=ins[a].at[_slot(peer)], dst_ref=outs[a].at[_slot(peer)],
                    send_sem=send_sems.at[N_REL * a + k - 1], recv_sem=recv_sems.at[N_REL * a + k - 1],
                    device_id=peer, device_id_type=MESH).wait_recv()
                i += 1
        for cp in copies:
            cp.wait_send()
        for cp in mine:
            cp.wait()

    return pl.pallas_call(
        body, name=name,
        in_specs=[ANY] * n, out_specs=[ANY] * n,
        out_shape=[jax.ShapeDtypeStruct(a.shape, a.dtype) for a in arrays],
        scratch_shapes=[pltpu.SemaphoreType.DMA((N_REL * n,)), pltpu.SemaphoreType.DMA((N_REL * n,)),
                        pltpu.SemaphoreType.DMA((n,))],
    )(*arrays)


HBM = pl.BlockSpec(memory_space=pltpu.HBM)
SEM = pl.BlockSpec(memory_space=pltpu.SEMAPHORE)
EFFECT = pltpu.SideEffectType.DATAFLOW_SIDE_EFFECTING


def _peer_of(x, y, c, k):
    flip = lambda v, bit: 1 - v if bit else v
    return flip(x, k & 4), flip(y, k & 2), flip(c, k & 1)


def _split_copies(gather, srcs, lands, send_sems, recv_sems, local_sems):
    x, y, c = _place()
    me = _slot((x, y, c))
    local, sends, waits = [], [], []
    for a, (src, land) in enumerate(zip(srcs, lands)):
        local.append(pltpu.make_async_copy(src if gather else src.at[me], land.at[me], local_sems.at[a]))
        for k in range(1, N_DEV):
            peer = _peer_of(x, y, c, k)
            mine = src if gather else src.at[_slot(peer)]
            sems = dict(send_sem=send_sems.at[N_REL * a + k - 1], recv_sem=recv_sems.at[N_REL * a + k - 1],
                        device_id=peer, device_id_type=MESH)
            sends.append(pltpu.make_async_remote_copy(src_ref=mine, dst_ref=land.at[me], **sems))
            waits.append(pltpu.make_async_remote_copy(src_ref=mine, dst_ref=land.at[_slot(peer)], **sems))
    return local, sends, waits


def _split_start(gather, srcs, name, after):
    n = len(srcs)
    land_shapes = [((N_DEV,) + s.shape) if gather else s.shape for s in srcs]
    lands = [lax.empty(shp, s.dtype) for shp, s in zip(land_shapes, srcs)]

    def body(*refs):
        src_refs, land_refs = refs[:n], refs[n:2 * n]
        send_sems, recv_sems, local_sems = refs[2 * n + 1:2 * n + 4]
        token = refs[-1]
        local, sends, _ = _split_copies(gather, src_refs, land_refs, send_sems, recv_sems, local_sems)
        for cp in local + sends:
            cp.start()
        token[...] = jnp.zeros_like(token)

    hbm = lambda a: pltpu.with_memory_space_constraint(a, pltpu.HBM)
    out = pl.pallas_call(
        body, name=name,
        out_shape=(pltpu.SemaphoreType.DMA((N_REL * n,)), pltpu.SemaphoreType.DMA((N_REL * n,)),
                   pltpu.SemaphoreType.DMA((n,)),
                   *[pltpu.HBM(s.shape, s.dtype) for s in srcs], *[pltpu.HBM(l.shape, l.dtype) for l in lands],
                   jax.ShapeDtypeStruct((8, DH), F32)),
        in_specs=[HBM] * (2 * n) + [ANY],
        out_specs=(SEM, SEM, SEM, *([HBM] * (2 * n)), pl.BlockSpec(memory_space=pltpu.VMEM)),
        input_output_aliases={i: 3 + i for i in range(2 * n)},
        compiler_params=pltpu.CompilerParams(has_side_effects=EFFECT),
    )(*[hbm(s) for s in srcs], *[hbm(l) for l in lands], after)
    return (gather, n, out[:3], out[3:3 + n], out[3 + n:3 + 2 * n]), out[-1]


def _split_wait(handle, name, after):
    gather, n, sems, srcs, lands = handle

    def body(*refs):
        src_refs, land_refs = refs[:n], refs[n:2 * n]
        send_sems, recv_sems, local_sems = refs[2 * n:2 * n + 3]
        local, _, waits = _split_copies(gather, src_refs, land_refs, send_sems, recv_sems, local_sems)
        for cp in waits:
            cp.wait_send()
            cp.wait_recv()
        for cp in local:
            cp.wait()

    out = pl.pallas_call(
        body, name=name,
        out_shape=(*[pltpu.HBM(s.shape, s.dtype) for s in srcs], *[pltpu.HBM(l.shape, l.dtype) for l in lands]),
        in_specs=[HBM] * (2 * n) + [SEM, SEM, SEM, ANY],
        out_specs=tuple([HBM] * (2 * n)),
        input_output_aliases={i: i for i in range(2 * n)},
        compiler_params=pltpu.CompilerParams(has_side_effects=EFFECT),
    )(*srcs, *lands, *sems, after)
    return list(out[n:])


def _tie(x, token, name):
    def body(x_ref, t_ref, o_ref):
        pass

    return pl.pallas_call(
        body, name=name, out_shape=jax.ShapeDtypeStruct(x.shape, x.dtype),
        in_specs=[ANY, ANY], out_specs=ANY, input_output_aliases={0: 0},
    )(x, token)


def _mod_fwd(a, w, b):
    def body(a_ref, w_ref, b_ref, o_ref):
        o_ref[...] = _dot(_silu(a_ref[...]), w_ref[...], NN, precision=HI) + b_ref[...]

    return pl.pallas_call(
        body, name="mod_fwd", out_shape=jax.ShapeDtypeStruct((a.shape[0], w.shape[1]), F32),
        compiler_params=pltpu.CompilerParams(vmem_limit_bytes=VMEM_LIMIT),
    )(a, w, b)


def _mod_bwd(a, d, w):
    def body(a_ref, d_ref, w_ref, dw_ref, dc_ref):
        av = a_ref[...]
        dv = d_ref[...]
        dw_ref[...] = _dot(_silu(av), dv, TN, precision=HI)
        da = _dot(dv[0:8, :], w_ref[...], NT, precision=HI) * _dsilu(av[0:8, :])
        row = lax.broadcasted_iota(jnp.int32, da.shape, 0)
        dc_ref[...] = jnp.where(row == 0, da, 0.0)

    return pl.pallas_call(
        body, name="mod_bwd",
        out_shape=[jax.ShapeDtypeStruct(w.shape, F32), jax.ShapeDtypeStruct((8, w.shape[0]), F32)],
        compiler_params=pltpu.CompilerParams(vmem_limit_bytes=VMEM_LIMIT),
    )(a, d, w)


def _sum_devices(g):
    def body(g_ref, o_ref):
        acc = g_ref[0]
        for i in range(1, g.shape[0]):
            acc = acc + g_ref[i]
        o_ref[...] = acc

    return pl.pallas_call(body, name="sum_devices_%d" % g.shape[1],
                          out_shape=jax.ShapeDtypeStruct(g.shape[1:], F32))(g)


def _adam_rows(r, c, n):
    budget = 6 * 1024 * 1024
    best = None
    for tr in range(16, r + 1, 16):
        if r % tr == 0 and tr * c * (2 * n + 28) <= budget:
            best = tr
    return best if best is not None else r


def _adamw(g, w, m, v, name):
    n, r, c = g.shape
    tr = _adam_rows(r, c, n)
    bc1 = 1.0 - ADAM_B1 ** ADAM_STEP
    bc2 = 1.0 - ADAM_B2 ** ADAM_STEP

    def body(g_ref, w_ref, m_ref, v_ref, go_ref, d_ref, mo_ref, vo_ref):
        grad = g_ref[0].astype(F32)
        for i in range(1, n):
            grad = grad + g_ref[i].astype(F32)
        go_ref[...] = grad
        m_new = ADAM_B1 * m_ref[...] + (1.0 - ADAM_B1) * grad
        v_new = ADAM_B2 * v_ref[...] + (1.0 - ADAM_B2) * (grad * grad)
        mo_ref[...] = m_new
        vo_ref[...] = v_new
        d_ref[...] = -ADAM_LR * ((m_new / bc1) / (jnp.sqrt(v_new / bc2) + ADAM_EPS) + ADAM_WD * w_ref[...])

    blk = pl.BlockSpec((tr, c), lambda i: (i, 0))
    out = jax.ShapeDtypeStruct((r, c), F32)
    return pl.pallas_call(
        body, name=name, grid=(r // tr,),
        in_specs=[pl.BlockSpec((n, tr, c), lambda i: (0, i, 0)), blk, blk, blk],
        out_specs=[blk] * 4, out_shape=[out] * 4,
        compiler_params=_cp("parallel"),
    )(g, w, m, v)


def kernel(x, c, ctx, c_ctx, w_mod, b_mod, norm_pre1, norm_post1, norm_pre2, norm_post2, w_in, hg_lb, hg_onorm, gla_w_gk, gla_b_gk, gla_onorm, w_br_hg, w_br_gla, w_out, w_ff_gate, w_ff_up, w_ff_down, loss_target, m_c_ctx, m_w_mod, m_b_mod, m_norm_pre1, m_norm_post1, m_norm_pre2, m_norm_post2, m_w_in, m_hg_lb, m_hg_onorm, m_gla_w_gk, m_gla_b_gk, m_gla_onorm, m_w_br_hg, m_w_br_gla, m_w_out, m_w_ff_gate, m_w_ff_up, m_w_ff_down, v_c_ctx, v_w_mod, v_b_mod, v_norm_pre1, v_norm_post1, v_norm_pre2, v_norm_post2, v_w_in, v_hg_lb, v_hg_onorm, v_gla_w_gk, v_gla_b_gk, v_gla_onorm, v_w_br_hg, v_w_br_gla, v_w_out, v_w_ff_gate, v_w_ff_up, v_w_ff_down):
    xi, yi, ci = lax.axis_index("x"), lax.axis_index("y"), lax.axis_index("c")
    me = 4 * xi + 2 * yi + ci
    t = CTX + x.shape[1]

    c_all, lb_g, wgk_g, bgk_g = _all_gather([c, hg_lb, gla_w_gk[0], gla_b_gk[0]], "ag_small")
    big = [w_in[0], w_br_hg[0], w_br_gla[0], w_out[0], w_ff_gate[0], w_ff_up[0], w_ff_down[0]]
    big_bf = [w.astype(BF16) for w in big]
    g_in, = _all_gather(big_bf[:1], "ag_w_in")
    mix_handle, tok = _split_start(True, big_bf[1:4], "ag_mix_start", g_in)
    ffn_handle, tok = _split_start(True, big_bf[4:], "ag_ffn_start", tok)
    cols = lambda g: jnp.transpose(g, (1, 0, 2)).reshape(g.shape[1], N_DEV * g.shape[2])
    w_in_k = _tie(_layout_w_in(cols(g_in)), tok, "tie_w_in")

    def get_mix(after):
        g_brh, g_brg, g_out = _split_wait(mix_handle, "ag_mix_wait", after)
        return cols(g_brh), cols(g_brg), g_out.reshape(D, D)

    def get_ffn(after):
        g_gate, g_up, g_down = _split_wait(ffn_handle, "ag_ffn_wait", after)
        return jnp.concatenate([cols(g_gate), cols(g_up)], axis=1), g_down.reshape(D_FF, D)

    hg_lb_full = jnp.transpose(lb_g, (1, 2, 0, 3)).reshape(2, 2, HW)
    wgk_k = _layout_wgk(jnp.transpose(wgk_g, (1, 2, 0, 3)).reshape(2, 16, HW)).astype(BF16)
    bgk_k = jnp.transpose(bgk_g, (1, 0, 2)).reshape(1, D)
    onw = jnp.concatenate([jnp.tile(hg_onorm, (1, NH // 2)), jnp.tile(gla_onorm, (1, NH // 2))], axis=1)

    n_mod = w_mod.shape[2]
    a9 = jnp.concatenate([c_ctx[None], c_all[:, 0], jnp.zeros((16 - 1 - N_DEV, D), F32)], axis=0)
    b_loc = lax.dynamic_slice(b_mod, (0, me * n_mod), (1, n_mod))
    s_loc = _mod_fwd(a9, w_mod[0], b_loc)
    s_all, = _all_gather([s_loc], "ag_mod")
    mod_all = jnp.transpose(s_all, (1, 0, 2)).reshape(16, N_DEV * n_mod)
    pad8 = lambda m: jnp.concatenate([m.reshape(6, D), jnp.zeros((2, D), F32)], axis=0)
    modc = pad8(mod_all[0])
    modx = pad8(lax.dynamic_slice(mod_all, (1 + me, 0), (1, N_DEV * n_mod))[0])

    z = jnp.concatenate([ctx[0], x[0]], axis=0)
    norms = (norm_pre1, norm_post1, norm_pre2, norm_post2)
    shard = lambda d: jnp.transpose(d.reshape(d.shape[0], N_DEV, -1), (1, 0, 2)).astype(BF16)
    rowshard = lambda d: d.reshape(N_DEV, d.shape[0] // N_DEV, d.shape[1]).astype(BF16)
    sent = []

    def send(names, grads, x_after):
        arrs, leaves = [], []
        for nm, g in zip(names, grads):
            if nm == "w_gu":
                arrs += [shard(g[:, :D_FF]), shard(g[:, D_FF:])]
                leaves += ["w_ff_gate", "w_ff_up"]
            elif nm == "w_in":
                arrs.append(shard(_unlayout_w_in(g)))
                leaves.append(nm)
            elif nm in ("w_out", "w_down"):
                arrs.append(rowshard(g))
                leaves.append({"w_down": "w_ff_down"}.get(nm, nm))
            else:
                arrs.append(shard(g))
                leaves.append(nm)
        handle, tok = _split_start(False, arrs, "grads_%s_start" % names[0], x_after)
        sent.append((names[0], leaves, handle))
        return _tie(x_after, tok, "tie_" + names[0])

    r = _local_step(z, loss_target[0], modc, modx, norms, onw, hg_lb_full, wgk_k, bgk_k,
                    w_in_k, get_mix, get_ffn, send)
    loss = lax.psum((0.5 / D) * jnp.sum(r["loss_vec"]), ("x", "y", "c"))
    grad_x = r["dz"][CTX:][None]

    sm_pre, sm_mid, sm_fin = r["sm_pre"], r["sm_mid"], r["sm_final"]
    dmodc = jnp.stack([sm_pre[0], sm_pre[2], sm_mid[4], sm_mid[0], sm_mid[2], sm_fin[0]]).reshape(-1)
    dmodx = jnp.stack([sm_pre[1], sm_pre[3], sm_mid[5], sm_mid[1], sm_mid[3], sm_fin[1]]).reshape(-1)
    on = r["sm_post"][0].reshape(NH, DH)
    pieces = [dmodc, dmodx, sm_pre[4], sm_mid[7], sm_mid[6], sm_fin[2], on[:NH // 2].sum(0), on[NH // 2:].sum(0),
              r["d_lb"][:2].reshape(-1), _unlayout_wgk(r["d_wgk"]).reshape(-1), r["d_bgk"][0]]
    sizes = [p.shape[0] for p in pieces]
    pack = jnp.concatenate(pieces).reshape(-1, DH)
    pack_all, = _all_gather([pack], "ag_small_grads")
    tot = _sum_devices(pack_all).reshape(-1)
    offs = [sum(sizes[:i]) for i in range(len(sizes))]
    part = lambda i: tot[offs[i]:offs[i] + sizes[i]]
    dmodc_t, dmodx_t = part(0), part(1)
    g_b_mod = (dmodc_t + dmodx_t)[None]
    g_norms = [part(i)[None] for i in (2, 3, 4, 5)]
    g_hg_on, g_gla_on = part(6)[None], part(7)[None]
    lb0 = lax.dynamic_slice(part(8).reshape(2, HW), (0, me * (HW // N_DEV)), (2, HW // N_DEV))
    g_hg_lb = jnp.stack([lb0, -lb0])
    g_wgk = lax.dynamic_slice(part(9).reshape(2, 16, HW), (0, 0, me * (HW // N_DEV)), (2, 16, HW // N_DEV))[None]
    g_bgk = lax.dynamic_slice(part(10).reshape(2, HW), (0, me * (HW // N_DEV)), (2, HW // N_DEV))[None]

    dmx_all = pack_all.reshape(N_DEV, -1)[:, sizes[0]:sizes[0] + sizes[1]]
    d9 = jnp.concatenate([lax.dynamic_slice(dmodc_t[None], (0, me * n_mod), (1, n_mod)),
                          lax.dynamic_slice(dmx_all, (0, me * n_mod), (N_DEV, n_mod)),
                          jnp.zeros((16 - 1 - N_DEV, n_mod), F32)], axis=0)
    g_w_mod, dcc_part = _mod_bwd(a9, d9, w_mod[0])
    dcc_all, = _all_gather([dcc_part], "ag_c_ctx")
    g_c_ctx = _sum_devices(dcc_all)[0]

    recv = {}
    for first, leaves, handle in sent:
        after = g_c_ctx if first == "w_in" else r["dz"]
        recv.update(zip(leaves, _split_wait(handle, "grads_%s_wait" % first, after)))
    moms = [(m_w_in, v_w_in), (m_w_br_hg, v_w_br_hg), (m_w_br_gla, v_w_br_gla), (m_w_out, v_w_out),
            (m_w_ff_gate, v_w_ff_gate), (m_w_ff_up, v_w_ff_up), (m_w_ff_down, v_w_ff_down)]
    names = ["w_in", "w_br_hg", "w_br_gla", "w_out", "w_ff_gate", "w_ff_up", "w_ff_down"]
    res = {}
    for nm, w, (m, v) in zip(names, big, moms):
        res[nm] = [o[None] for o in _adamw(recv[nm], w, m[0], v[0], "adamw_" + nm)]
    res["w_mod"] = [o[None] for o in _adamw(g_w_mod[None], w_mod[0], m_w_mod[0], v_w_mod[0], "adamw_w_mod")]

    small = [("c_ctx", c_ctx, m_c_ctx, v_c_ctx, g_c_ctx), ("b_mod", b_mod, m_b_mod, v_b_mod, g_b_mod),
             ("norm_pre1", norm_pre1, m_norm_pre1, v_norm_pre1, g_norms[0]),
             ("norm_post1", norm_post1, m_norm_post1, v_norm_post1, g_norms[1]),
             ("norm_pre2", norm_pre2, m_norm_pre2, v_norm_pre2, g_norms[2]),
             ("norm_post2", norm_post2, m_norm_post2, v_norm_post2, g_norms[3]),
             ("hg_lb", hg_lb, m_hg_lb, v_hg_lb, g_hg_lb), ("hg_onorm", hg_onorm, m_hg_onorm, v_hg_onorm, g_hg_on),
             ("gla_w_gk", gla_w_gk, m_gla_w_gk, v_gla_w_gk, g_wgk), ("gla_b_gk", gla_b_gk, m_gla_b_gk, v_gla_b_gk, g_bgk),
             ("gla_onorm", gla_onorm, m_gla_onorm, v_gla_onorm, g_gla_on)]
    flat = lambda k: jnp.concatenate([s[k].reshape(-1) for s in small]).reshape(-1, DH)
    outs = _adamw(flat(4)[None], flat(1), flat(2), flat(3), "adamw_small")
    off = 0
    for nm, w, _, _, _ in small:
        res[nm] = [o.reshape(-1)[off:off + w.size].reshape(w.shape) for o in outs]
        off += w.size

    order = ["c_ctx", "w_mod", "b_mod", "norm_pre1", "norm_post1", "norm_pre2", "norm_post2", "w_in", "hg_lb",
             "hg_onorm", "gla_w_gk", "gla_b_gk", "gla_onorm", "w_br_hg", "w_br_gla", "w_out", "w_ff_gate", "w_ff_up",
             "w_ff_down"]
    return (loss, grad_x, *[res[n][k] for k in range(4) for n in order])
```

```python
import functools

import jax
import jax.numpy as jnp
from jax import lax
from jax.experimental import pallas as pl
from jax.experimental.pallas import tpu as pltpu

F32 = jnp.float32
BF16 = jnp.bfloat16
HI = lax.Precision.HIGHEST

N_DEV = 8
D = 1024
CTX = 256
HW = 512
DH = 128
NH = 8
D_FF = 2816
EPS = 1e-6
GLA_NORM = 16.0
CHUNK = 64
TR = 256
NCT = CTX // TR
W_IN_COLS = 7168
MAIN0 = 2048
LR0 = 6656
LEVELS = (32, 16, 8)
EXP_CLAMP = 80.0
VMEM_LIMIT = 48 * 1024 * 1024

ADAM_LR, ADAM_B1, ADAM_B2, ADAM_EPS, ADAM_WD, ADAM_STEP = 0.001, 0.9, 0.999, 1e-08, 0.01, 10


def _cp(*sem):
    return pltpu.CompilerParams(dimension_semantics=sem, vmem_limit_bytes=VMEM_LIMIT)


def _sig(x):
    return jax.nn.sigmoid(x)


def _silu(x):
    return x * _sig(x)


def _dsilu(x):
    s = _sig(x)
    return s * (1.0 + x * (1.0 - s))


def _rstd(x):
    return lax.rsqrt(jnp.mean(x * x, axis=-1, keepdims=True) + EPS)


def _rms_bwd(a, y, r):
    return r * (a - y * (r * r) * jnp.mean(a * y, axis=-1, keepdims=True))


def _colsum(x):
    return jnp.sum(x, axis=0, keepdims=True)


def _dot(a, b, dims, precision=None):
    return lax.dot_general(a, b, (dims, ((), ())), preferred_element_type=F32, precision=precision)


NN = ((1,), (0,))
NT = ((1,), (1,))
TN = ((0,), (0,))

SCAN_HEADS_FWD = 4
SCAN_HEADS_BWD = 4


def _split_dot(m, x):
    mb = m.astype(BF16)
    x1 = x.astype(BF16)
    r1 = x - x1.astype(F32)
    x2 = r1.astype(BF16)
    x3 = (r1 - x2.astype(F32)).astype(BF16)
    return _dot(mb, x1, NN) + _dot(mb, x2, NN) + _dot(mb, x3, NN)


def _matmul(a, b, dims, out_dtype, name, tm, tn, tk, a_off=0, m_out=None):
    if dims == NN:
        m, k = a.shape[0], b.shape[0]
        n = b.shape[1]
        a_spec = pl.BlockSpec((tm, tk), lambda i, j, kk: (i, kk + a_off))
        b_spec = pl.BlockSpec((tk, tn), lambda i, j, kk: (kk, j))
    elif dims == NT:
        m, k = a.shape[0], b.shape[1]
        n = b.shape[0]
        a_spec = pl.BlockSpec((tm, tk), lambda i, j, kk: (i, kk + a_off))
        b_spec = pl.BlockSpec((tn, tk), lambda i, j, kk: (j, kk))
    else:
        m, k = (a.shape[1] if m_out is None else m_out), a.shape[0]
        n = b.shape[1]
        a_spec = pl.BlockSpec((tk, tm), lambda i, j, kk: (kk, i + a_off))
        b_spec = pl.BlockSpec((tk, tn), lambda i, j, kk: (kk, j))
    assert m % tm == 0 and n % tn == 0 and k % tk == 0, (name, m, n, k, tm, tn, tk)
    nk = k // tk

    def body(a_ref, b_ref, o_ref, *acc):
        part = _dot(a_ref[...], b_ref[...], dims)
        if nk == 1:
            o_ref[...] = part.astype(o_ref.dtype)
            return
        acc_ref, = acc
        kk = pl.program_id(2)

        @pl.when(kk == 0)
        def _():
            acc_ref[...] = part

        @pl.when(kk > 0)
        def _():
            acc_ref[...] += part

        @pl.when(kk == nk - 1)
        def _():
            o_ref[...] = acc_ref[...].astype(o_ref.dtype)

    return pl.pallas_call(
        body,
        name=name,
        grid=(m // tm, n // tn, nk),
        in_specs=[a_spec, b_spec],
        out_specs=pl.BlockSpec((tm, tn), lambda i, j, kk: (i, j)),
        out_shape=jax.ShapeDtypeStruct((m, n), out_dtype),
        scratch_shapes=[] if nk == 1 else [pltpu.VMEM((tm, tn), F32)],
        compiler_params=_cp("parallel", "parallel", "arbitrary"),
    )(a, b)


def _row(c):
    return pl.BlockSpec((TR, c), lambda i: (i, 0))


def _rowcol(width, cb):
    return pl.BlockSpec((TR, width), lambda i: (i, cb))


def _full(shape):
    return pl.BlockSpec(shape, lambda i: (0,) * len(shape))


def _mod_row(mc_ref, mx_ref, k, is_ctx):
    return jnp.where(is_ctx, mc_ref[k:k + 1, :], mx_ref[k:k + 1, :])


def _acc_row(ref, k, val):
    ref[k:k + 1, :] += val


def _acc_mod(ref, k, is_ctx, val):
    zero = jnp.zeros_like(val)
    ref[k:k + 1, :] += jnp.where(is_ctx, val, zero)
    ref[k + 1:k + 2, :] += jnp.where(is_ctx, zero, val)


def _prenorm(z, nw, modc, modx, i_shift, i_scale, name):
    t = z.shape[0]

    def body(z_ref, nw_ref, mc_ref, mx_ref, h_ref):
        is_ctx = pl.program_id(0) < NCT
        x = z_ref[...]
        n = x * _rstd(x) * nw_ref[...]
        h = n * (1.0 + _mod_row(mc_ref, mx_ref, i_scale, is_ctx)) + _mod_row(mc_ref, mx_ref, i_shift, is_ctx)
        h_ref[...] = h.astype(BF16)

    return pl.pallas_call(
        body, name=name, grid=(t // TR,),
        in_specs=[_row(D), _full((1, D)), _full((8, D)), _full((8, D))],
        out_specs=_row(D),
        out_shape=jax.ShapeDtypeStruct((t, D), BF16),
        compiler_params=_cp("parallel"),
    )(z, nw, modc, modx)


def _hg_lb(lb_ref, d):
    a0 = lb_ref[0, d:d + 1, :]
    a1 = lb_ref[1, d:d + 1, :]
    mx = jnp.maximum(a0, a1)
    e0 = jnp.exp(a0 - mx)
    e1 = jnp.exp(a1 - mx)
    return e0 / (e0 + e1)


def _log_sigmoid(x):
    return jnp.minimum(x, 0.0) - jnp.log(1.0 + jnp.exp(-jnp.abs(x)))


def _gates_fwd(p, hg_lb, wgk, bgk):
    t = p.shape[0]
    seg = lambda j: _rowcol(HW, MAIN0 // HW + j)

    def body(hq_ref, hi_ref, hf_ref, hb_ref, gq_ref, gk_ref, gv_ref, lr_ref, lb_ref, wgk_ref, bgk_ref,
             q_ref, v_ref, kf_ref, kb_ref, gf_ref, gb_ref):
        q_ref[:, :HW] = _silu(hq_ref[...].astype(F32))
        q_ref[:, HW:] = gq_ref[...].astype(F32) * (DH ** -0.5)
        v_ref[:, :HW] = hi_ref[...].astype(F32)
        v_ref[:, HW:] = gv_ref[...].astype(F32)
        xg = _dot(lr_ref[...].astype(BF16), wgk_ref[...], NN) + bgk_ref[...]
        for d, (raw_ref, k_ref, g_ref) in enumerate(((hf_ref, kf_ref, gf_ref), (hb_ref, kb_ref, gb_ref))):
            lbd = _hg_lb(lb_ref, d)
            f = lbd + (1.0 - lbd) * _sig(raw_ref[...].astype(F32))
            k_ref[:, :HW] = 1.0 - f
            k_ref[:, HW:] = gk_ref[...].astype(F32)
            g_ref[:, :HW] = jnp.log(f)
            g_ref[:, HW:] = _log_sigmoid(xg[:, d * HW:(d + 1) * HW]) * (1.0 / GLA_NORM)

    out = jax.ShapeDtypeStruct((t, D), F32)
    return pl.pallas_call(
        body, name="gates_fwd", grid=(t // TR,),
        in_specs=[seg(0), seg(1), seg(2), seg(3), seg(5), seg(6), seg(7), _rowcol(DH, LR0 // DH),
                  _full((2, 2, HW)), _full((DH, D)), _full((1, D))],
        out_specs=[_row(D)] * 6,
        out_shape=[out] * 6,
        compiler_params=_cp("parallel"),
    )(p, p, p, p, p, p, p, p, hg_lb, wgk, bgk)


def _post_fwd(o_fw, o_bw, p, onw):
    t = o_fw.shape[0]

    def body(of_ref, ob_ref, g1_ref, g2_ref, w_ref, y_ref):
        for h in range(NH):
            sl = slice(h * DH, (h + 1) * DH)
            o = of_ref[:, sl] + ob_ref[:, sl]
            g_ref = g1_ref if h < NH // 2 else g2_ref
            gs = slice((h % (NH // 2)) * DH, (h % (NH // 2) + 1) * DH)
            n = o * _rstd(o) * w_ref[:, sl]
            y_ref[:, sl] = (n * _silu(g_ref[:, gs].astype(F32))).astype(BF16)

    return pl.pallas_call(
        body, name="post_fwd", grid=(t // TR,),
        in_specs=[_row(D), _row(D), _rowcol(HW, MAIN0 // HW + 4), _rowcol(HW, MAIN0 // HW + 8), _full((1, D))],
        out_specs=_row(D),
        out_shape=jax.ShapeDtypeStruct((t, D), BF16),
        compiler_params=_cp("parallel"),
    )(o_fw, o_bw, p, p, onw)


def _merge_fwd(p, u1, u2):
    t = p.shape[0]

    def body(g1_ref, g2_ref, u1_ref, u2_ref, m_ref):
        f = lambda r: r[...].astype(F32)
        m_ref[...] = (_sig(f(g1_ref)) * f(u1_ref) + _sig(f(g2_ref)) * f(u2_ref)).astype(BF16)

    return pl.pallas_call(
        body, name="merge_fwd", grid=(t // TR,),
        in_specs=[_rowcol(D, 0), _rowcol(D, 1), _row(D), _row(D)],
        out_specs=_row(D),
        out_shape=jax.ShapeDtypeStruct((t, D), BF16),
        compiler_params=_cp("parallel"),
    )(p, p, u1, u2)


def _mid_fwd(z, y1, nw_post, nw_pre, modc, modx):
    t = z.shape[0]

    def body(z_ref, y_ref, wpo_ref, wpr_ref, mc_ref, mx_ref, z1_ref, h_ref):
        is_ctx = pl.program_id(0) < NCT
        y = y_ref[...]
        z1 = z_ref[...] + _mod_row(mc_ref, mx_ref, 2, is_ctx) * (y * _rstd(y) * wpo_ref[...])
        z1_ref[...] = z1
        n = z1 * _rstd(z1) * wpr_ref[...]
        h = n * (1.0 + _mod_row(mc_ref, mx_ref, 4, is_ctx)) + _mod_row(mc_ref, mx_ref, 3, is_ctx)
        h_ref[...] = h.astype(BF16)

    return pl.pallas_call(
        body, name="mid_fwd", grid=(t // TR,),
        in_specs=[_row(D), _row(D), _full((1, D)), _full((1, D)), _full((8, D)), _full((8, D))],
        out_specs=[_row(D), _row(D)],
        out_shape=[jax.ShapeDtypeStruct((t, D), F32), jax.ShapeDtypeStruct((t, D), BF16)],
        compiler_params=_cp("parallel"),
    )(z, y1, nw_post, nw_pre, modc, modx)


def _swiglu_fwd(uv):
    t = uv.shape[0]

    def body(u_ref, v_ref, a_ref):
        a_ref[...] = (_silu(u_ref[...].astype(F32)) * v_ref[...].astype(F32)).astype(BF16)

    return pl.pallas_call(
        body, name="swiglu_fwd", grid=(t // TR,),
        in_specs=[_rowcol(D_FF, 0), _rowcol(D_FF, 1)],
        out_specs=_row(D_FF),
        out_shape=jax.ShapeDtypeStruct((t, D_FF), BF16),
        compiler_params=_cp("parallel"),
    )(uv, uv)


def _swiglu_bwd(uv, da):
    t = uv.shape[0]

    def body(u_ref, v_ref, da_ref, d_ref):
        u = u_ref[...].astype(F32)
        d = da_ref[...].astype(F32)
        d_ref[:, :D_FF] = (d * v_ref[...].astype(F32) * _dsilu(u)).astype(BF16)
        d_ref[:, D_FF:] = (d * _silu(u)).astype(BF16)

    return pl.pallas_call(
        body, name="swiglu_bwd", grid=(t // TR,),
        in_specs=[_rowcol(D_FF, 0), _rowcol(D_FF, 1), _row(D_FF)],
        out_specs=_row(2 * D_FF),
        out_shape=jax.ShapeDtypeStruct((t, 2 * D_FF), BF16),
        compiler_params=_cp("parallel"),
    )(uv, uv, da)


def _final(z1, y2, target, nw, modc, modx):
    t = z1.shape[0]

    def body(z1_ref, y_ref, tg_ref, w_ref, mc_ref, mx_ref, dz_ref, dy_ref, loss_ref, sm_ref):
        i = pl.program_id(0)
        is_ctx = i < NCT

        @pl.when(i == 0)
        def _():
            loss_ref[...] = jnp.zeros_like(loss_ref)
            sm_ref[...] = jnp.zeros_like(sm_ref)

        g = _mod_row(mc_ref, mx_ref, 5, is_ctx)
        y = y_ref[...]
        r = _rstd(y)
        w = w_ref[...]
        yr = y * r
        n = yr * w
        e = z1_ref[...] + g * n - tg_ref[...]
        lat = jnp.where(is_ctx, 0.0, 1.0)
        loss_ref[...] += lat * _colsum(e * e)
        dz = e * (lat / D)
        dz_ref[...] = dz
        _acc_mod(sm_ref, 0, is_ctx, _colsum(dz * n))
        dn = dz * g
        _acc_row(sm_ref, 2, _colsum(dn * yr))
        dy_ref[...] = _rms_bwd(dn * w, y, r).astype(BF16)

    return pl.pallas_call(
        body, name="final", grid=(t // TR,),
        in_specs=[_row(D), _row(D), pl.BlockSpec((TR, D), lambda i: (jnp.maximum(i - NCT, 0), 0)),
                  _full((1, D)), _full((8, D)), _full((8, D))],
        out_specs=[_row(D), _row(D), _full((1, D)), _full((8, D))],
        out_shape=[jax.ShapeDtypeStruct((t, D), F32), jax.ShapeDtypeStruct((t, D), BF16),
                   jax.ShapeDtypeStruct((1, D), F32), jax.ShapeDtypeStruct((8, D), F32)],
        compiler_params=_cp("arbitrary"),
    )(z1, y2, target, nw, modc, modx)


def _mid_bwd(dh2, dz, z, z1, y1, nw_post, nw_pre, modc, modx):
    t = z.shape[0]

    def body(dh_ref, dz_ref, z_ref, z1_ref, y_ref, wpo_ref, wpr_ref, mc_ref, mx_ref, dzo_ref, dy_ref, sm_ref):
        i = pl.program_id(0)
        is_ctx = i < NCT

        @pl.when(i == 0)
        def _():
            sm_ref[...] = jnp.zeros_like(sm_ref)

        dh = dh_ref[...]
        z1 = z1_ref[...]
        r = _rstd(z1)
        zr = z1 * r
        wpr = wpr_ref[...]
        n = zr * wpr
        _acc_mod(sm_ref, 0, is_ctx, _colsum(dh))
        _acc_mod(sm_ref, 2, is_ctx, _colsum(dh * n))
        dn = dh * (1.0 + _mod_row(mc_ref, mx_ref, 4, is_ctx))
        _acc_row(sm_ref, 6, _colsum(dn * zr))
        dz1 = dz_ref[...] + _rms_bwd(dn * wpr, z1, r)
        dzo_ref[...] = dz1
        y = y_ref[...]
        r1 = _rstd(y)
        yr = y * r1
        wpo = wpo_ref[...]
        g = _mod_row(mc_ref, mx_ref, 2, is_ctx)
        _acc_mod(sm_ref, 4, is_ctx, _colsum(dz1 * (yr * wpo)))
        dn1 = dz1 * g
        _acc_row(sm_ref, 7, _colsum(dn1 * yr))
        dy_ref[...] = _rms_bwd(dn1 * wpo, y, r1).astype(BF16)

    return pl.pallas_call(
        body, name="mid_bwd", grid=(t // TR,),
        in_specs=[_row(D)] * 5 + [_full((1, D)), _full((1, D)), _full((8, D)), _full((8, D))],
        out_specs=[_row(D), _row(D), _full((8, D))],
        out_shape=[jax.ShapeDtypeStruct((t, D), F32), jax.ShapeDtypeStruct((t, D), BF16),
                   jax.ShapeDtypeStruct((8, D), F32)],
        compiler_params=_cp("arbitrary"),
    )(dh2, dz, z, z1, y1, nw_post, nw_pre, modc, modx)


def _pre_bwd(dh1, dz, z, nw, modc, modx):
    t = z.shape[0]

    def body(dh_ref, dz_ref, z_ref, w_ref, mc_ref, mx_ref, dzo_ref, sm_ref):
        i = pl.program_id(0)
        is_ctx = i < NCT

        @pl.when(i == 0)
        def _():
            sm_ref[...] = jnp.zeros_like(sm_ref)

        dh = dh_ref[...]
        x = z_ref[...]
        r = _rstd(x)
        xr = x * r
        w = w_ref[...]
        _acc_mod(sm_ref, 0, is_ctx, _colsum(dh))
        _acc_mod(sm_ref, 2, is_ctx, _colsum(dh * (xr * w)))
        dn = dh * (1.0 + _mod_row(mc_ref, mx_ref, 1, is_ctx))
        _acc_row(sm_ref, 4, _colsum(dn * xr))
        dzo_ref[...] = dz_ref[...] + _rms_bwd(dn * w, x, r)

    return pl.pallas_call(
        body, name="pre_bwd", grid=(t // TR,),
        in_specs=[_row(D)] * 3 + [_full((1, D)), _full((8, D)), _full((8, D))],
        out_specs=[_row(D), _full((8, D))],
        out_shape=[jax.ShapeDtypeStruct((t, D), F32), jax.ShapeDtypeStruct((8, D), F32)],
        compiler_params=_cp("arbitrary"),
    )(dh1, dz, z, nw, modc, modx)


def _merge_bwd(dm, p, u1, u2):
    t = dm.shape[0]

    def body(dm_ref, g1_ref, g2_ref, u1_ref, u2_ref, du1_ref, du2_ref, dg_ref):
        dm_ = dm_ref[...]
        s1 = _sig(g1_ref[...].astype(F32))
        s2 = _sig(g2_ref[...].astype(F32))
        du1_ref[...] = (dm_ * s1).astype(BF16)
        du2_ref[...] = (dm_ * s2).astype(BF16)
        dg_ref[:, :D] = (dm_ * u1_ref[...].astype(F32) * s1 * (1.0 - s1)).astype(BF16)
        dg_ref[:, D:] = (dm_ * u2_ref[...].astype(F32) * s2 * (1.0 - s2)).astype(BF16)

    return pl.pallas_call(
        body, name="merge_bwd", grid=(t // TR,),
        in_specs=[_row(D), _rowcol(D, 0), _rowcol(D, 1), _row(D), _row(D)],
        out_specs=[_row(D), _row(D), _row(2 * D)],
        out_shape=[jax.ShapeDtypeStruct((t, D), BF16), jax.ShapeDtypeStruct((t, D), BF16),
                   jax.ShapeDtypeStruct((t, 2 * D), BF16)],
        compiler_params=_cp("parallel"),
    )(dm, p, p, u1, u2)


def _post_bwd(dy_hg, dy_gla, o_fw, o_bw, p, onw):
    t = o_fw.shape[0]

    def body(d1_ref, d2_ref, of_ref, ob_ref, g1_ref, g2_ref, w_ref, do_ref, dg_ref, sm_ref):
        @pl.when(pl.program_id(0) == 0)
        def _():
            sm_ref[...] = jnp.zeros_like(sm_ref)

        for h in range(NH):
            sl = slice(h * DH, (h + 1) * DH)
            gs = slice((h % (NH // 2)) * DH, (h % (NH // 2) + 1) * DH)
            g_ref, d_ref = (g1_ref, d1_ref) if h < NH // 2 else (g2_ref, d2_ref)
            o = of_ref[:, sl] + ob_ref[:, sl]
            r = _rstd(o)
            orr = o * r
            w = w_ref[:, sl]
            gt = g_ref[:, gs].astype(F32)
            dy = d_ref[:, gs]
            dg_ref[:, sl] = (dy * (orr * w) * _dsilu(gt)).astype(BF16)
            dn = dy * _silu(gt)
            sm_ref[0:1, sl] += _colsum(dn * orr)
            do_ref[:, sl] = _rms_bwd(dn * w, o, r)

    return pl.pallas_call(
        body, name="post_bwd", grid=(t // TR,),
        in_specs=[_row(HW), _row(HW), _row(D), _row(D), _rowcol(HW, MAIN0 // HW + 4), _rowcol(HW, MAIN0 // HW + 8),
                  _full((1, D))],
        out_specs=[_row(D), _row(D), _full((8, D))],
        out_shape=[jax.ShapeDtypeStruct((t, D), F32), jax.ShapeDtypeStruct((t, D), BF16),
                   jax.ShapeDtypeStruct((8, D), F32)],
        compiler_params=_cp("arbitrary"),
    )(dy_hg, dy_gla, o_fw, o_bw, p, p, onw)


def _gates_bwd(p, hg_lb, wgk, bgk, dgm, dgo, dq_f, dq_b, dv_f, dv_b, dk_f, dk_b, dg_f, dg_b):
    t = p.shape[0]
    seg = lambda j: _rowcol(HW, MAIN0 // HW + j)

    def body(hq_ref, hf_ref, hb_ref, lr_ref, lb_ref, wgk_ref, bgk_ref, dgm_ref, dgo_ref,
             dqf_ref, dqb_ref, dvf_ref, dvb_ref, dkf_ref, dkb_ref, dgf_ref, dgb_ref,
             dp_ref, dlb_ref, dw_ref, db_ref):
        @pl.when(pl.program_id(0) == 0)
        def _():
            dlb_ref[...] = jnp.zeros_like(dlb_ref)
            dw_ref[...] = jnp.zeros_like(dw_ref)
            db_ref[...] = jnp.zeros_like(db_ref)

        c0 = MAIN0

        def put(j, val):
            dp_ref[:, c0 + j * HW:c0 + (j + 1) * HW] = val.astype(BF16)

        dp_ref[:, :MAIN0] = dgm_ref[...]
        dq = dqf_ref[...] + dqb_ref[...]
        dv = dvf_ref[...] + dvb_ref[...]
        put(0, dq[:, :HW] * _dsilu(hq_ref[...].astype(F32)))
        put(1, dv[:, :HW])
        put(5, dq[:, HW:] * (DH ** -0.5))
        put(7, dv[:, HW:])
        put(6, dkf_ref[:, HW:] + dkb_ref[:, HW:])
        dp_ref[:, c0 + 4 * HW:c0 + 5 * HW] = dgo_ref[:, :HW]
        dp_ref[:, c0 + 8 * HW:c0 + 9 * HW] = dgo_ref[:, HW:]
        lr = lr_ref[...].astype(BF16)
        xg = _dot(lr, wgk_ref[...], NN) + bgk_ref[...]
        dxg = []
        for d, (raw_ref, dk_ref, dg_ref) in enumerate(((hf_ref, dkf_ref, dgf_ref), (hb_ref, dkb_ref, dgb_ref))):
            lbd = _hg_lb(lb_ref, d)
            s = _sig(raw_ref[...].astype(F32))
            f = lbd + (1.0 - lbd) * s
            df = dg_ref[:, :HW] / f - dk_ref[:, :HW]
            put(2 + d, df * (1.0 - lbd) * s * (1.0 - s))
            dlb_ref[d:d + 1, :] += _colsum(df * (1.0 - s)) * (lbd * (1.0 - lbd))
            dxg.append(dg_ref[:, HW:] * (1.0 / GLA_NORM) * _sig(-xg[:, d * HW:(d + 1) * HW]))
        dxg = jnp.concatenate(dxg, axis=1)
        db_ref[0:1, :] += _colsum(dxg)
        dxg_b = dxg.astype(BF16)
        dw_ref[...] += _dot(lr, dxg_b, TN)
        dp_ref[:, LR0:LR0 + DH] = _dot(dxg_b, wgk_ref[...], NT).astype(BF16)
        dp_ref[:, LR0 + DH:] = jnp.zeros((TR, W_IN_COLS - LR0 - DH), BF16)

    return pl.pallas_call(
        body, name="gates_bwd", grid=(t // TR,),
        in_specs=[seg(0), seg(2), seg(3), _rowcol(DH, LR0 // DH), _full((2, 2, HW)), _full((DH, D)), _full((1, D)),
                  _row(2 * D), _row(D)] + [_row(D)] * 8,
        out_specs=[_row(W_IN_COLS), _full((8, HW)), _full((DH, D)), _full((8, D))],
        out_shape=[jax.ShapeDtypeStruct((t, W_IN_COLS), BF16), jax.ShapeDtypeStruct((8, HW), F32),
                   jax.ShapeDtypeStruct((DH, D), F32), jax.ShapeDtypeStruct((8, D), F32)],
        compiler_params=_cp("arbitrary"),
    )(p, p, p, p, hg_lb, wgk, bgk, dgm, dgo, dq_f, dq_b, dv_f, dv_b, dk_f, dk_b, dg_f, dg_b)


def _scan_consts(rev):
    r = lax.broadcasted_iota(jnp.int32, (CHUNK, CHUNK), 0)
    u = lax.broadcasted_iota(jnp.int32, (CHUNK, CHUNK), 1)
    rp = lax.broadcasted_iota(jnp.int32, (CHUNK, 1), 0)
    if rev:
        r, u, rp = CHUNK - 1 - r, CHUNK - 1 - u, CHUNK - 1 - rp
    tri = jnp.where(u <= r, 1.0, 0.0).astype(F32)
    tri_t = jnp.where(r <= u, 1.0, 0.0).astype(F32)
    lv = []
    for b in LEVELS:
        sh = b.bit_length() - 1
        pair = ((r >> sh) == (u >> sh) + 1) & (((u >> sh) & 1) == 0)
        pair_t = ((u >> sh) == (r >> sh) + 1) & (((r >> sh) & 1) == 0)
        tside = ((rp >> sh) & 1) == 1
        lv.append((pair, pair_t, tside))
    bd = LEVELS[-1].bit_length() - 1
    diag = ((r >> bd) == (u >> bd)) & (u <= r)
    diag_t = ((r >> bd) == (u >> bd)) & (r <= u)
    return tri, tri_t, lv, diag, diag_t


def _row_of(pos, rev):
    return CHUNK - 1 - pos if rev else pos


def _chunk_terms(cum, b_scr, consts, rev):
    _, _, lv, _, _ = consts
    terms = []
    for b, (_, _, tside) in zip(LEVELS, lv):
        pieces = []
        for j in range(CHUNK // (2 * b)):
            row = _row_of(2 * b * j + b - 1, rev)
            pieces.append(jnp.broadcast_to(b_scr[row:row + 1, :], (2 * b, DH)))
        if rev:
            pieces = pieces[::-1]
        bnd = pieces[0] if len(pieces) == 1 else jnp.concatenate(pieces, axis=0)
        w = jnp.exp(jnp.minimum(jnp.where(tside, cum - bnd, bnd - cum), 0.0))
        wq = jnp.where(tside, w, 0.0)
        wk = jnp.where(tside, 0.0, w)
        terms.append((wq, wk))
    b = LEVELS[-1]
    pieces = []
    for j in range(CHUNK // b):
        if j == 0:
            pieces.append(jnp.zeros((b, DH), F32))
        else:
            row = _row_of(b * j - 1, rev)
            pieces.append(jnp.broadcast_to(b_scr[row:row + 1, :], (b, DH)))
    if rev:
        pieces = pieces[::-1]
    start = jnp.concatenate(pieces, axis=0)
    wq = jnp.exp(jnp.minimum(cum - start, 0.0))
    wk = jnp.exp(jnp.minimum(start - cum, EXP_CLAMP))
    terms.append((wq, wk))
    return terms


def _run_staged(units):
    live = list(units)
    while live:
        nxt = []
        for u in live:
            try:
                next(u)
                nxt.append(u)
            except StopIteration:
                pass
        live = nxt


SCAN_TB = 256
SCAN_CB = SCAN_TB // CHUNK


def _block_order(i, ntb, rev):
    nctx = CTX // SCAN_TB
    if not rev:
        return i
    return jnp.where(i < nctx, nctx - 1 - i, ntb - 1 - (i - nctx))


def _chunk_in_block(j, rev):
    return SCAN_CB - 1 - j if rev else j


def _scan_fwd(q, k, v, g, rev):
    t = q.shape[0]
    nc = t // CHUNK
    hpb = SCAN_HEADS_FWD

    def body(q_ref, k_ref, v_ref, g_ref, o_ref, st_ref, s_scr, b_scr):
        consts = _scan_consts(rev)
        _, _, lv, diag, _ = consts
        masks = [pair for pair, _, _ in lv] + [diag]

        @pl.when(pl.program_id(1) == 0)
        def _():
            s_scr[...] = jnp.zeros_like(s_scr)

        tri = consts[0]
        state = {hh: s_scr[hh] for hh in range(hpb)}

        def unit(hh, j):
            sl = slice(hh * DH, (hh + 1) * DH)
            c = _chunk_in_block(j, rev)
            rows = slice(c * CHUNK, (c + 1) * CHUNK)
            b_ref = b_scr.at[hh * SCAN_CB + j]
            qc, kc, vc, gc = q_ref[rows, sl], k_ref[rows, sl], v_ref[rows, sl], g_ref[rows, sl]
            cum = _split_dot(tri, gc)
            b_ref[...] = cum
            yield
            terms = _chunk_terms(cum, b_ref, consts, rev)
            ops = [((qc * wq).astype(BF16), (kc * wk).astype(BF16)) for wq, wk in terms]
            tot = _colsum(gc)
            qe = (qc * jnp.exp(cum)).astype(BF16)
            ke = (kc * jnp.exp(tot - cum)).astype(BF16)
            vb = vc.astype(BF16)
            yield
            scs = [_dot(qt, kt, NT) for qt, kt in ops]
            kv = _dot(vb, ke, TN)
            yield
            a = jnp.zeros((CHUNK, CHUNK), F32)
            for sc, m in zip(scs, masks):
                a = a + jnp.where(m, sc, 0.0)
            o_intra = _dot(a.astype(BF16), vb, NN)
            yield
            st = state[hh]
            st_ref[hh, c] = st
            o_ref[rows, sl] = o_intra + _dot(qe, st.astype(BF16), NT)
            state[hh] = st * jnp.exp(tot) + kv
            yield

        _run_staged([unit(hh, j) for hh in range(hpb) for j in range(SCAN_CB)])
        for hh in range(hpb):
            s_scr[hh] = state[hh]

    ntb = t // SCAN_TB
    col = pl.BlockSpec((SCAN_TB, hpb * DH), lambda h, i: (_block_order(i, ntb, rev), h))
    return pl.pallas_call(
        body, name="scan_fwd_" + ("bw" if rev else "fw"), grid=(NH // hpb, ntb),
        in_specs=[col] * 4,
        out_specs=[col, pl.BlockSpec((hpb, SCAN_CB, DH, DH), lambda h, i: (h, _block_order(i, ntb, rev), 0, 0))],
        out_shape=[jax.ShapeDtypeStruct((t, D), F32), jax.ShapeDtypeStruct((NH, nc, DH, DH), F32)],
        scratch_shapes=[pltpu.VMEM((hpb, DH, DH), F32), pltpu.VMEM((hpb * SCAN_CB, CHUNK, DH), F32)],
        compiler_params=_cp("parallel", "arbitrary"),
    )(q, k, v, g)


def _scan_bwd(q, k, v, g, do, states, rev):
    t = q.shape[0]
    nc = t // CHUNK
    hpb = SCAN_HEADS_BWD

    def body(q_ref, k_ref, v_ref, g_ref, do_ref, st_ref, dq_ref, dk_ref, dv_ref, dg_ref, ds_scr, b_scr):
        consts = _scan_consts(rev)
        _, tri_t, lv, diag, diag_t = consts
        masks = [(pair, pair_t) for pair, pair_t, _ in lv] + [(diag, diag_t)]
        @pl.when(pl.program_id(1) == 0)
        def _():
            ds_scr[...] = jnp.zeros_like(ds_scr)

        tri = consts[0]
        dstate = {hh: ds_scr[hh] for hh in range(hpb)}

        def unit(hh, jj):
            sl = slice(hh * DH, (hh + 1) * DH)
            c = _chunk_in_block(SCAN_CB - 1 - jj, rev)
            rows = slice(c * CHUNK, (c + 1) * CHUNK)
            b_ref = b_scr.at[hh * SCAN_CB + jj]
            qc, kc, vc, gc = q_ref[rows, sl], k_ref[rows, sl], v_ref[rows, sl], g_ref[rows, sl]
            dob = do_ref[rows, sl].astype(BF16)
            vb = vc.astype(BF16)
            cum = _split_dot(tri, gc)
            b_ref[...] = cum
            da = _dot(dob, vb, NT)
            da_t = _dot(vb, dob, NT)
            yield
            terms = _chunk_terms(cum, b_ref, consts, rev)
            ops = [((qc * wq).astype(BF16), (kc * wk).astype(BF16)) for wq, wk in terms]
            tot = _colsum(gc)
            e_tot = jnp.exp(tot)
            e_b = jnp.exp(cum)
            e_t = jnp.exp(tot - cum)
            qeb = (qc * e_b).astype(BF16)
            keb = (kc * e_t).astype(BF16)
            dal = [(jnp.where(m, da, 0.0).astype(BF16), jnp.where(m_t, da_t, 0.0).astype(BF16)) for m, m_t in masks]
            yield
            ats = [_dot(ktb, qtb, NT) for qtb, ktb in ops]
            dqts = [_dot(d, ktb, NN) for (d, _), (_, ktb) in zip(dal, ops)]
            dkts = [_dot(d_t, qtb, NN) for (_, d_t), (qtb, _) in zip(dal, ops)]
            qd = _dot(dob, qeb, TN)
            yield
            a_t = jnp.zeros((CHUNK, CHUNK), F32)
            dq = jnp.zeros((CHUNK, DH), F32)
            dk = jnp.zeros((CHUNK, DH), F32)
            db = jnp.zeros((CHUNK, DH), F32)
            for at, dqt, dkt, (wq, wk), (qtb, ktb), (_, m_t) in zip(ats, dqts, dkts, terms, ops, masks):
                a_t = a_t + jnp.where(m_t, at, 0.0)
                dq = dq + dqt * wq
                dk = dk + dkt * wk
                db = db + dqt * qtb.astype(F32) - dkt * ktb.astype(F32)
            dv_intra = _dot(a_t.astype(BF16), dob, NN)
            st = st_ref[hh, c]
            stb = st.astype(BF16)
            dqe = _dot(dob, stb, NN)
            yield
            dst = dstate[hh]
            dstb = dst.astype(BF16)
            dstate[hh] = dst * e_tot + qd
            dv_ref[rows, sl] = dv_intra + _dot(keb, dstb, NT)
            dke = _dot(vb, dstb, NN)
            yield
            qe = qeb.astype(F32)
            ke = keb.astype(F32)
            dq_ref[rows, sl] = dq + dqe * e_b
            dk_ref[rows, sl] = dk + dke * e_t
            db = db + dqe * qe - dke * ke
            dtot = _colsum(dstb.astype(F32) * stb.astype(F32)) * e_tot + _colsum(dke * ke)
            dg_ref[rows, sl] = _split_dot(tri_t, db) + dtot
            yield

        _run_staged([unit(hh, jj) for hh in range(hpb) for jj in range(SCAN_CB)])
        for hh in range(hpb):
            ds_scr[hh] = dstate[hh]

    ntb = t // SCAN_TB
    blk = lambda i: _block_order(ntb - 1 - i, ntb, rev)
    col = pl.BlockSpec((SCAN_TB, hpb * DH), lambda h, i: (blk(i), h))
    out = jax.ShapeDtypeStruct((t, D), F32)
    return pl.pallas_call(
        body, name="scan_bwd_" + ("bw" if rev else "fw"), grid=(NH // hpb, ntb),
        in_specs=[col] * 5 + [pl.BlockSpec((hpb, SCAN_CB, DH, DH), lambda h, i: (h, blk(i), 0, 0))],
        out_specs=[col] * 4,
        out_shape=[out] * 4,
        scratch_shapes=[pltpu.VMEM((hpb, DH, DH), F32), pltpu.VMEM((hpb * SCAN_CB, CHUNK, DH), F32)],
        compiler_params=_cp("parallel", "arbitrary"),
    )(q, k, v, g, do, states)


W_IN_REF = 6688
GATE0 = 4640
LRW = 32


def _layout_w_in(w):
    return jnp.concatenate([w[:, GATE0:], w[:, :GATE0 - LRW], w[:, GATE0 - LRW:GATE0],
                            jnp.zeros((w.shape[0], W_IN_COLS - W_IN_REF), w.dtype)], axis=1)


def _unlayout_w_in(d):
    return jnp.concatenate([d[:, MAIN0:LR0 + LRW], d[:, :MAIN0]], axis=1)


def _layout_wgk(w):
    r = w.shape[1]
    top = jnp.concatenate([w[0], jnp.zeros_like(w[0])], axis=1)
    bot = jnp.concatenate([jnp.zeros_like(w[1]), w[1]], axis=1)
    return jnp.concatenate([top, bot, jnp.zeros((DH - 2 * r, D), w.dtype)], axis=0)


def _unlayout_wgk(d, r=16):
    return jnp.stack([d[:r, :HW], d[r:2 * r, HW:]])


def _local_step(z, target, modc, modx, norms, onw, hg_lb, wgk, bgk, w_in, get_mix, get_ffn, send):
    n_pre1, n_post1, n_pre2, n_post2 = norms
    t = z.shape[0]
    tm = 768 if t % 768 == 0 else 256
    h1 = _prenorm(z, n_pre1, modc, modx, 0, 1, "prenorm1")
    p = _matmul(h1, w_in, NN, BF16, "mm_in", tm, 512, D)
    q, v, k_f, k_b, g_f, g_b = _gates_fwd(p, hg_lb, wgk, bgk)
    o_f, st_f = _scan_fwd(q, k_f, v, g_f, False)
    o_b, st_b = _scan_fwd(q, k_b, v, g_b, True)
    y = _post_fwd(o_f, o_b, p, onw)
    w_br_hg, w_br_gla, w_out = get_mix(y)
    u1 = _matmul(y, w_br_hg, NN, BF16, "mm_br_hg", tm, 512, HW, a_off=0)
    u2 = _matmul(y, w_br_gla, NN, BF16, "mm_br_gla", tm, 512, HW, a_off=1)
    merged = _merge_fwd(p, u1, u2)
    y1 = _matmul(merged, w_out, NN, F32, "mm_out", tm, 512, D)
    z1, h2 = _mid_fwd(z, y1, n_post1, n_pre2, modc, modx)
    w_gu, w_down = get_ffn(h2)
    uv = _matmul(h2, w_gu, NN, BF16, "mm_gu", tm, 512, D)
    act = _swiglu_fwd(uv)
    y2 = _matmul(act, w_down, NN, F32, "mm_down", tm, 512, D_FF // 2)
    dz, dy2, loss_vec, sm_final = _final(z1, y2, target, n_post2, modc, modx)
    dact = _matmul(dy2, w_down, NT, BF16, "mm_down_dx", tm, D_FF // 2, D)
    d_w_down = _matmul(act, dy2, TN, BF16, "mm_down_dw", D_FF // 2, 512, t)
    duv = _swiglu_bwd(uv, dact)
    dh2 = _matmul(duv, w_gu, NT, F32, "mm_gu_dx", tm, 512, D_FF // 2)
    d_w_gu = _matmul(h2, duv, TN, BF16, "mm_gu_dw", 512, 512, t)
    dh2 = send(("w_down", "w_gu"), (d_w_down, d_w_gu), dh2)
    dz, dy1, sm_mid = _mid_bwd(dh2, dz, z, z1, y1, n_post1, n_pre2, modc, modx)
    dmerged = _matmul(dy1, w_out, NT, F32, "mm_out_dx", tm, 512, D)
    d_w_out = _matmul(merged, dy1, TN, BF16, "mm_out_dw", 512, 512, t)
    du1, du2, dgm = _merge_bwd(dmerged, p, u1, u2)
    dy_hg = _matmul(du1, w_br_hg, NT, F32, "mm_br_hg_dx", tm, HW, D)
    dy_gla = _matmul(du2, w_br_gla, NT, F32, "mm_br_gla_dx", tm, HW, D)
    d_w_br_hg = _matmul(y, du1, TN, BF16, "mm_br_hg_dw", HW, 512, t, a_off=0, m_out=HW)
    d_w_br_gla = _matmul(y, du2, TN, BF16, "mm_br_gla_dw", HW, 512, t, a_off=1, m_out=HW)
    dy_hg = send(("w_out", "w_br_hg", "w_br_gla"), (d_w_out, d_w_br_hg, d_w_br_gla), dy_hg)
    do, dgo, sm_post = _post_bwd(dy_hg, dy_gla, o_f, o_b, p, onw)
    dq_f, dk_f, dv_f, dg_f = _scan_bwd(q, k_f, v, g_f, do, st_f, False)
    dq_b, dk_b, dv_b, dg_b = _scan_bwd(q, k_b, v, g_b, do, st_b, True)
    dp, d_lb, d_wgk, d_bgk = _gates_bwd(p, hg_lb, wgk, bgk, dgm, dgo, dq_f, dq_b, dv_f, dv_b, dk_f, dk_b, dg_f, dg_b)
    d_w_in = _matmul(h1, dp, TN, BF16, "mm_in_dw", 512, 512, t)
    dp = send(("w_in",), (d_w_in,), dp)
    dh1 = _matmul(dp, w_in, NT, F32, "mm_in_dx", tm, 512, 1024)
    dz, sm_pre = _pre_bwd(dh1, dz, z, n_pre1, modc, modx)
    return dict(loss_vec=loss_vec, dz=dz, sm_final=sm_final, sm_mid=sm_mid, sm_post=sm_post, sm_pre=sm_pre,
                d_lb=d_lb, d_wgk=d_wgk, d_bgk=d_bgk)


MESH = pl.DeviceIdType.MESH
ANY = pl.BlockSpec(memory_space=pl.ANY)
N_REL = N_DEV - 1


def _place():
    return lax.axis_index("x"), lax.axis_index("y"), lax.axis_index("c")


def _slot(p):
    return 4 * p[0] + 2 * p[1] + p[2]


def _all_gather(arrays, name):
    n = len(arrays)

    def body(*refs):
        ins, outs = refs[:n], refs[n:2 * n]
        send_sems, recv_sems, local_sems = refs[2 * n:]
        x, y, c = _place()
        me, sibling = (x, y, c), (x, y, 1 - c)
        chips = [(1 - x, y), (x, 1 - y), (1 - x, 1 - y)]

        def copy(a, k, block, to, src=None):
            dst = outs[a].at[_slot(block)]
            return pltpu.make_async_remote_copy(
                src_ref=dst if src is None else src, dst_ref=dst,
                send_sem=send_sems.at[N_REL * a + k], recv_sem=recv_sems.at[N_REL * a + k],
                device_id=to, device_id_type=MESH)

        mine = [pltpu.make_async_copy(ins[a], outs[a].at[_slot(me)], local_sems.at[a]) for a in range(n)]
        for cp in mine:
            cp.start()
        first = []
        for a in range(n):
            first.append(copy(a, 0, me, sibling, src=ins[a]))
            first += [copy(a, 1 + j, me, (*chip, c), src=ins[a]) for j, chip in enumerate(chips)]
        for cp in first:
            cp.start()
        passed = []
        for j, chip in enumerate(chips):
            for a in range(n):
                copy(a, 1 + j, (*chip, c), me).wait_recv()
                fwd = copy(a, 4 + j, (*chip, c), sibling)
                fwd.start()
                passed.append(fwd)
        for a in range(n):
            copy(a, 0, sibling, me).wait_recv()
        for j, chip in enumerate(chips):
            for a in range(n):
                copy(a, 4 + j, (*chip, 1 - c), me).wait_recv()
        for cp in first + passed:
            cp.wait_send()
        for cp in mine:
            cp.wait()

    return pl.pallas_call(
        body, name=name,
        in_specs=[ANY] * n, out_specs=[ANY] * n,
        out_shape=[jax.ShapeDtypeStruct((N_DEV,) + a.shape, a.dtype) for a in arrays],
        scratch_shapes=[pltpu.SemaphoreType.DMA((N_REL * n,)), pltpu.SemaphoreType.DMA((N_REL * n,)),
                        pltpu.SemaphoreType.DMA((n,))],
    )(*arrays)


def _exchange(arrays, name):
    n = len(arrays)

    def body(*refs):
        ins, outs = refs[:n], refs[n:2 * n]
        send_sems, recv_sems, local_sems = refs[2 * n:]
        x, y, c = _place()
        me = _slot((x, y, c))
        mine = [pltpu.make_async_copy(ins[a].at[me], outs[a].at[me], local_sems.at[a]) for a in range(n)]
        for cp in mine:
            cp.start()
        copies = []
        for a in range(n):
            for k in range(1, N_DEV):
                flip = lambda v, bit: 1 - v if bit else v
                peer = (flip(x, k & 4), flip(y, k & 2), flip(c, k & 1))
                copies.append(pltpu.make_async_remote_copy(
                    src_ref=ins[a].at[_slot(peer)], dst_ref=outs[a].at[me],
                    send_sem=send_sems.at[N_REL * a + k - 1], recv_sem=recv_sems.at[N_REL * a + k - 1],
                    device_id=peer, device_id_type=MESH))
                copies[-1].start()
        i = 0
        for a in range(n):
            for k in range(1, N_DEV):
                flip = lambda v, bit: 1 - v if bit else v
                peer = (flip(x, k & 4), flip(y, k & 2), flip(c, k & 1))
                pltpu.make_async_remote_copy(
                    src_ref=ins[a].at[_slot(peer)], dst_ref=outs[a].at[_slot(peer)],
                    send_sem=send_sems.at[N_REL * a + k - 1], recv_sem=recv_sems.at[N_REL * a + k - 1],
                    device_id=peer, device_id_type=MESH).wait_recv()
                i += 1
        for cp in copies:
            cp.wait_send()
        for cp in mine:
            cp.wait()

    return pl.pallas_call(
        body, name=name,
        in_specs=[ANY] * n, out_specs=[ANY] * n,
        out_shape=[jax.ShapeDtypeStruct(a.shape, a.dtype) for a in arrays],
        scratch_shapes=[pltpu.SemaphoreType.DMA((N_REL * n,)), pltpu.SemaphoreType.DMA((N_REL * n,)),
                        pltpu.SemaphoreType.DMA((n,))],
    )(*arrays)


HBM = pl.BlockSpec(memory_space=pltpu.HBM)
SEM = pl.BlockSpec(memory_space=pltpu.SEMAPHORE)
EFFECT = pltpu.SideEffectType.DATAFLOW_SIDE_EFFECTING


def _peer_of(x, y, c, k):
    flip = lambda v, bit: 1 - v if bit else v
    return flip(x, k & 4), flip(y, k & 2), flip(c, k & 1)


def _split_copies(gather, srcs, lands, send_sems, recv_sems, local_sems):
    x, y, c = _place()
    me = _slot((x, y, c))
    local, sends, waits = [], [], []
    for a, (src, land) in enumerate(zip(srcs, lands)):
        local.append(pltpu.make_async_copy(src if gather else src.at[me], land.at[me], local_sems.at[a]))
        for k in range(1, N_DEV):
            peer = _peer_of(x, y, c, k)
            mine = src if gather else src.at[_slot(peer)]
            sems = dict(send_sem=send_sems.at[N_REL * a + k - 1], recv_sem=recv_sems.at[N_REL * a + k - 1],
                        device_id=peer, device_id_type=MESH)
            sends.append(pltpu.make_async_remote_copy(src_ref=mine, dst_ref=land.at[me], **sems))
            waits.append(pltpu.make_async_remote_copy(src_ref=mine, dst_ref=land.at[_slot(peer)], **sems))
    return local, sends, waits


def _split_start(gather, srcs, name, after):
    n = len(srcs)
    land_shapes = [((N_DEV,) + s.shape) if gather else s.shape for s in srcs]
    lands = [lax.empty(shp, s.dtype) for shp, s in zip(land_shapes, srcs)]

    def body(*refs):
        src_refs, land_refs = refs[:n], refs[n:2 * n]
        send_sems, recv_sems, local_sems = refs[2 * n + 1:2 * n + 4]
        token = refs[-1]
        local, sends, _ = _split_copies(gather, src_refs, land_refs, send_sems, recv_sems, local_sems)
        for cp in local + sends:
            cp.start()
        token[...] = jnp.zeros_like(token)

    hbm = lambda a: pltpu.with_memory_space_constraint(a, pltpu.HBM)
    out = pl.pallas_call(
        body, name=name,
        out_shape=(pltpu.SemaphoreType.DMA((N_REL * n,)), pltpu.SemaphoreType.DMA((N_REL * n,)),
                   pltpu.SemaphoreType.DMA((n,)),
                   *[pltpu.HBM(s.shape, s.dtype) for s in srcs], *[pltpu.HBM(l.shape, l.dtype) for l in lands],
                   jax.ShapeDtypeStruct((8, DH), F32)),
        in_specs=[HBM] * (2 * n) + [ANY],
        out_specs=(SEM, SEM, SEM, *([HBM] * (2 * n)), pl.BlockSpec(memory_space=pltpu.VMEM)),
        input_output_aliases={i: 3 + i for i in range(2 * n)},
        compiler_params=pltpu.CompilerParams(has_side_effects=EFFECT),
    )(*[hbm(s) for s in srcs], *[hbm(l) for l in lands], after)
    return (gather, n, out[:3], out[3:3 + n], out[3 + n:3 + 2 * n]), out[-1]


def _split_wait(handle, name, after):
    gather, n, sems, srcs, lands = handle

    def body(*refs):
        src_refs, land_refs = refs[:n], refs[n:2 * n]
        send_sems, recv_sems, local_sems = refs[2 * n:2 * n + 3]
        local, _, waits = _split_copies(gather, src_refs, land_refs, send_sems, recv_sems, local_sems)
        for cp in waits:
            cp.wait_send()
            cp.wait_recv()
        for cp in local:
            cp.wait()

    out = pl.pallas_call(
        body, name=name,
        out_shape=(*[pltpu.HBM(s.shape, s.dtype) for s in srcs], *[pltpu.HBM(l.shape, l.dtype) for l in lands]),
        in_specs=[HBM] * (2 * n) + [SEM, SEM, SEM, ANY],
        out_specs=tuple([HBM] * (2 * n)),
        input_output_aliases={i: i for i in range(2 * n)},
        compiler_params=pltpu.CompilerParams(has_side_effects=EFFECT),
    )(*srcs, *lands, *sems, after)
    return list(out[n:])


def _tie(x, token, name):
    def body(x_ref, t_ref, o_ref):
        pass

    return pl.pallas_call(
        body, name=name, out_shape=jax.ShapeDtypeStruct(x.shape, x.dtype),
        in_specs=[ANY, ANY], out_specs=ANY, input_output_aliases={0: 0},
    )(x, token)


def _mod_fwd(a, w, b):
    def body(a_ref, w_ref, b_ref, o_ref):
        o_ref[...] = _dot(_silu(a_ref[...]), w_ref[...], NN, precision=HI) + b_ref[...]

    return pl.pallas_call(
        body, name="mod_fwd", out_shape=jax.ShapeDtypeStruct((a.shape[0], w.shape[1]), F32),
        compiler_params=pltpu.CompilerParams(vmem_limit_bytes=VMEM_LIMIT),
    )(a, w, b)


def _mod_bwd(a, d, w):
    def body(a_ref, d_ref, w_ref, dw_ref, dc_ref):
        av = a_ref[...]
        dv = d_ref[...]
        dw_ref[...] = _dot(_silu(av), dv, TN, precision=HI)
        da = _dot(dv[0:8, :], w_ref[...], NT, precision=HI) * _dsilu(av[0:8, :])
        row = lax.broadcasted_iota(jnp.int32, da.shape, 0)
        dc_ref[...] = jnp.where(row == 0, da, 0.0)

    return pl.pallas_call(
        body, name="mod_bwd",
        out_shape=[jax.ShapeDtypeStruct(w.shape, F32), jax.ShapeDtypeStruct((8, w.shape[0]), F32)],
        compiler_params=pltpu.CompilerParams(vmem_limit_bytes=VMEM_LIMIT),
    )(a, d, w)


def _sum_devices(g):
    def body(g_ref, o_ref):
        acc = g_ref[0]
        for i in range(1, g.shape[0]):
            acc = acc + g_ref[i]
        o_ref[...] = acc

    return pl.pallas_call(body, name="sum_devices_%d" % g.shape[1],
                          out_shape=jax.ShapeDtypeStruct(g.shape[1:], F32))(g)


def _adam_rows(r, c, n):
    budget = 6 * 1024 * 1024
    best = None
    for tr in range(16, r + 1, 16):
        if r % tr == 0 and tr * c * (2 * n + 28) <= budget:
            best = tr
    return best if best is not None else r


def _adamw(g, w, m, v, name):
    n, r, c = g.shape
    tr = _adam_rows(r, c, n)
    bc1 = 1.0 - ADAM_B1 ** ADAM_STEP
    bc2 = 1.0 - ADAM_B2 ** ADAM_STEP

    def body(g_ref, w_ref, m_ref, v_ref, go_ref, d_ref, mo_ref, vo_ref):
        grad = g_ref[0].astype(F32)
        for i in range(1, n):
            grad = grad + g_ref[i].astype(F32)
        go_ref[...] = grad
        m_new = ADAM_B1 * m_ref[...] + (1.0 - ADAM_B1) * grad
        v_new = ADAM_B2 * v_ref[...] + (1.0 - ADAM_B2) * (grad * grad)
        mo_ref[...] = m_new
        vo_ref[...] = v_new
        d_ref[...] = -ADAM_LR * ((m_new / bc1) / (jnp.sqrt(v_new / bc2) + ADAM_EPS) + ADAM_WD * w_ref[...])

    blk = pl.BlockSpec((tr, c), lambda i: (i, 0))
    out = jax.ShapeDtypeStruct((r, c), F32)
    return pl.pallas_call(
        body, name=name, grid=(r // tr,),
        in_specs=[pl.BlockSpec((n, tr, c), lambda i: (0, i, 0)), blk, blk, blk],
        out_specs=[blk] * 4, out_shape=[out] * 4,
        compiler_params=_cp("parallel"),
    )(g, w, m, v)


def kernel(x, c, ctx, c_ctx, w_mod, b_mod, norm_pre1, norm_post1, norm_pre2, norm_post2, w_in, hg_lb, hg_onorm, gla_w_gk, gla_b_gk, gla_onorm, w_br_hg, w_br_gla, w_out, w_ff_gate, w_ff_up, w_ff_down, loss_target, m_c_ctx, m_w_mod, m_b_mod, m_norm_pre1, m_norm_post1, m_norm_pre2, m_norm_post2, m_w_in, m_hg_lb, m_hg_onorm, m_gla_w_gk, m_gla_b_gk, m_gla_onorm, m_w_br_hg, m_w_br_gla, m_w_out, m_w_ff_gate, m_w_ff_up, m_w_ff_down, v_c_ctx, v_w_mod, v_b_mod, v_norm_pre1, v_norm_post1, v_norm_pre2, v_norm_post2, v_w_in, v_hg_lb, v_hg_onorm, v_gla_w_gk, v_gla_b_gk, v_gla_onorm, v_w_br_hg, v_w_br_gla, v_w_out, v_w_ff_gate, v_w_ff_up, v_w_ff_down):
    xi, yi, ci = lax.axis_index("x"), lax.axis_index("y"), lax.axis_index("c")
    me = 4 * xi + 2 * yi + ci
    t = CTX + x.shape[1]

    c_all, lb_g, wgk_g, bgk_g = _all_gather([c, hg_lb, gla_w_gk[0], gla_b_gk[0]], "ag_small")
    big = [w_in[0], w_br_hg[0], w_br_gla[0], w_out[0], w_ff_gate[0], w_ff_up[0], w_ff_down[0]]
    big_bf = [w.astype(BF16) for w in big]
    g_in, = _all_gather(big_bf[:1], "ag_w_in")
    mix_handle, tok = _split_start(True, big_bf[1:4], "ag_mix_start", g_in)
    ffn_handle, tok = _split_start(True, big_bf[4:], "ag_ffn_start", tok)
    cols = lambda g: jnp.transpose(g, (1, 0, 2)).reshape(g.shape[1], N_DEV * g.shape[2])
    w_in_k = _tie(_layout_w_in(cols(g_in)), tok, "tie_w_in")

    def get_mix(after):
        g_brh, g_brg, g_out = _split_wait(mix_handle, "ag_mix_wait", after)
        return cols(g_brh), cols(g_brg), g_out.reshape(D, D)

    def get_ffn(after):
        g_gate, g_up, g_down = _split_wait(ffn_handle, "ag_ffn_wait", after)
        return jnp.concatenate([cols(g_gate), cols(g_up)], axis=1), g_down.reshape(D_FF, D)

    hg_lb_full = jnp.transpose(lb_g, (1, 2, 0, 3)).reshape(2, 2, HW)
    wgk_k = _layout_wgk(jnp.transpose(wgk_g, (1, 2, 0, 3)).reshape(2, 16, HW)).astype(BF16)
    bgk_k = jnp.transpose(bgk_g, (1, 0, 2)).reshape(1, D)
    onw = jnp.concatenate([jnp.tile(hg_onorm, (1, NH // 2)), jnp.tile(gla_onorm, (1, NH // 2))], axis=1)

    n_mod = w_mod.shape[2]
    a9 = jnp.concatenate([c_ctx[None], c_all[:, 0], jnp.zeros((16 - 1 - N_DEV, D), F32)], axis=0)
    b_loc = lax.dynamic_slice(b_mod, (0, me * n_mod), (1, n_mod))
    s_loc = _mod_fwd(a9, w_mod[0], b_loc)
    s_all, = _all_gather([s_loc], "ag_mod")
    mod_all = jnp.transpose(s_all, (1, 0, 2)).reshape(16, N_DEV * n_mod)
    pad8 = lambda m: jnp.concatenate([m.reshape(6, D), jnp.zeros((2, D), F32)], axis=0)
    modc = pad8(mod_all[0])
    modx = pad8(lax.dynamic_slice(mod_all, (1 + me, 0), (1, N_DEV * n_mod))[0])

    z = jnp.concatenate([ctx[0], x[0]], axis=0)
    norms = (norm_pre1, norm_post1, norm_pre2, norm_post2)
    shard = lambda d: jnp.transpose(d.reshape(d.shape[0], N_DEV, -1), (1, 0, 2)).astype(BF16)
    rowshard = lambda d: d.reshape(N_DEV, d.shape[0] // N_DEV, d.shape[1]).astype(BF16)
    sent = []

    def send(names, grads, x_after):
        arrs, leaves = [], []
        for nm, g in zip(names, grads):
            if nm == "w_gu":
                arrs += [shard(g[:, :D_FF]), shard(g[:, D_FF:])]
                leaves += ["w_ff_gate", "w_ff_up"]
            elif nm == "w_in":
                arrs.append(shard(_unlayout_w_in(g)))
                leaves.append(nm)
            elif nm in ("w_out", "w_down"):
                arrs.append(rowshard(g))
                leaves.append({"w_down": "w_ff_down"}.get(nm, nm))
            else:
                arrs.append(shard(g))
                leaves.append(nm)
        handle, tok = _split_start(False, arrs, "grads_%s_start" % names[0], x_after)
        sent.append((names[0], leaves, handle))
        return _tie(x_after, tok, "tie_" + names[0])

    r = _local_step(z, loss_target[0], modc, modx, norms, onw, hg_lb_full, wgk_k, bgk_k,
                    w_in_k, get_mix, get_ffn, send)
    loss = lax.psum((0.5 / D) * jnp.sum(r["loss_vec"]), ("x", "y", "c"))
    grad_x = r["dz"][CTX:][None]

    sm_pre, sm_mid, sm_fin = r["sm_pre"], r["sm_mid"], r["sm_final"]
    dmodc = jnp.stack([sm_pre[0], sm_pre[2], sm_mid[4], sm_mid[0], sm_mid[2], sm_fin[0]]).reshape(-1)
    dmodx = jnp.stack([sm_pre[1], sm_pre[3], sm_mid[5], sm_mid[1], sm_mid[3], sm_fin[1]]).reshape(-1)
    on = r["sm_post"][0].reshape(NH, DH)
    pieces = [dmodc, dmodx, sm_pre[4], sm_mid[7], sm_mid[6], sm_fin[2], on[:NH // 2].sum(0), on[NH // 2:].sum(0),
              r["d_lb"][:2].reshape(-1), _unlayout_wgk(r["d_wgk"]).reshape(-1), r["d_bgk"][0]]
    sizes = [p.shape[0] for p in pieces]
    pack = jnp.concatenate(pieces).reshape(-1, DH)
    pack_all, = _all_gather([pack], "ag_small_grads")
    tot = _sum_devices(pack_all).reshape(-1)
    offs = [sum(sizes[:i]) for i in range(len(sizes))]
    part = lambda i: tot[offs[i]:offs[i] + sizes[i]]
    dmodc_t, dmodx_t = part(0), part(1)
    g_b_mod = (dmodc_t + dmodx_t)[None]
    g_norms = [part(i)[None] for i in (2, 3, 4, 5)]
    g_hg_on, g_gla_on = part(6)[None], part(7)[None]
    lb0 = lax.dynamic_slice(part(8).reshape(2, HW), (0, me * (HW // N_DEV)), (2, HW // N_DEV))
    g_hg_lb = jnp.stack([lb0, -lb0])
    g_wgk = lax.dynamic_slice(part(9).reshape(2, 16, HW), (0, 0, me * (HW // N_DEV)), (2, 16, HW // N_DEV))[None]
    g_bgk = lax.dynamic_slice(part(10).reshape(2, HW), (0, me * (HW // N_DEV)), (2, HW // N_DEV))[None]

    dmx_all = pack_all.reshape(N_DEV, -1)[:, sizes[0]:sizes[0] + sizes[1]]
    d9 = jnp.concatenate([lax.dynamic_slice(dmodc_t[None], (0, me * n_mod), (1, n_mod)),
                          lax.dynamic_slice(dmx_all, (0, me * n_mod), (N_DEV, n_mod)),
                          jnp.zeros((16 - 1 - N_DEV, n_mod), F32)], axis=0)
    g_w_mod, dcc_part = _mod_bwd(a9, d9, w_mod[0])
    dcc_all, = _all_gather([dcc_part], "ag_c_ctx")
    g_c_ctx = _sum_devices(dcc_all)[0]

    recv = {}
    for first, leaves, handle in sent:
        after = g_c_ctx if first == "w_in" else r["dz"]
        recv.update(zip(leaves, _split_wait(handle, "grads_%s_wait" % first, after)))
    moms = [(m_w_in, v_w_in), (m_w_br_hg, v_w_br_hg), (m_w_br_gla, v_w_br_gla), (m_w_out, v_w_out),
            (m_w_ff_gate, v_w_ff_gate), (m_w_ff_up, v_w_ff_up), (m_w_ff_down, v_w_ff_down)]
    names = ["w_in", "w_br_hg", "w_br_gla", "w_out", "w_ff_gate", "w_ff_up", "w_ff_down"]
    res = {}
    for nm, w, (m, v) in zip(names, big, moms):
        res[nm] = [o[None] for o in _adamw(recv[nm], w, m[0], v[0], "adamw_" + nm)]
    res["w_mod"] = [o[None] for o in _adamw(g_w_mod[None], w_mod[0], m_w_mod[0], v_w_mod[0], "adamw_w_mod")]

    small = [("c_ctx", c_ctx, m_c_ctx, v_c_ctx, g_c_ctx), ("b_mod", b_mod, m_b_mod, v_b_mod, g_b_mod),
             ("norm_pre1", norm_pre1, m_norm_pre1, v_norm_pre1, g_norms[0]),
             ("norm_post1", norm_post1, m_norm_post1, v_norm_post1, g_norms[1]),
             ("norm_pre2", norm_pre2, m_norm_pre2, v_norm_pre2, g_norms[2]),
             ("norm_post2", norm_post2, m_norm_post2, v_norm_post2, g_norms[3]),
             ("hg_lb", hg_lb, m_hg_lb, v_hg_lb, g_hg_lb), ("hg_onorm", hg_onorm, m_hg_onorm, v_hg_onorm, g_hg_on),
             ("gla_w_gk", gla_w_gk, m_gla_w_gk, v_gla_w_gk, g_wgk), ("gla_b_gk", gla_b_gk, m_gla_b_gk, v_gla_b_gk, g_bgk),
             ("gla_onorm", gla_onorm, m_gla_onorm, v_gla_onorm, g_gla_on)]
    flat = lambda k: jnp.concatenate([s[k].reshape(-1) for s in small]).reshape(-1, DH)
    outs = _adamw(flat(4)[None], flat(1), flat(2), flat(3), "adamw_small")
    off = 0
    for nm, w, _, _, _ in small:
        res[nm] = [o.reshape(-1)[off:off + w.size].reshape(w.shape) for o in outs]
        off += w.size

    order = ["c_ctx", "w_mod", "b_mod", "norm_pre1", "norm_post1", "norm_pre2", "norm_post2", "w_in", "hg_lb",
             "hg_onorm", "gla_w_gk", "gla_b_gk", "gla_onorm", "w_br_hg", "w_br_gla", "w_out", "w_ff_gate", "w_ff_up",
             "w_ff_down"]
    return (loss, grad_x, *[res[n][k] for k in range(4) for n in order])
```

```python
import functools

import jax
import jax.numpy as jnp
from jax import lax
from jax.experimental import pallas as pl
from jax.experimental.pallas import tpu as pltpu

F32 = jnp.float32
BF16 = jnp.bfloat16
HI = lax.Precision.HIGHEST

N_DEV = 8
D = 1024
CTX = 256
HW = 512
DH = 128
NH = 8
D_FF = 2816
EPS = 1e-6
GLA_NORM = 16.0
CHUNK = 64
TR = 256
NCT = CTX // TR
W_IN_COLS = 7168
MAIN0 = 2048
LR0 = 6656
LEVELS = (32, 16, 8)
EXP_CLAMP = 80.0
VMEM_LIMIT = 48 * 1024 * 1024

ADAM_LR, ADAM_B1, ADAM_B2, ADAM_EPS, ADAM_WD, ADAM_STEP = 0.001, 0.9, 0.999, 1e-08, 0.01, 10


def _cp(*sem):
    return pltpu.CompilerParams(dimension_semantics=sem, vmem_limit_bytes=VMEM_LIMIT)


def _sig(x):
    return jax.nn.sigmoid(x)


def _silu(x):
    return x * _sig(x)


def _dsilu(x):
    s = _sig(x)
    return s * (1.0 + x * (1.0 - s))


def _rstd(x):
    return lax.rsqrt(jnp.mean(x * x, axis=-1, keepdims=True) + EPS)


def _rms_bwd(a, y, r):
    return r * (a - y * (r * r) * jnp.mean(a * y, axis=-1, keepdims=True))


def _colsum(x):
    return jnp.sum(x, axis=0, keepdims=True)


def _dot(a, b, dims, precision=None):
    return lax.dot_general(a, b, (dims, ((), ())), preferred_element_type=F32, precision=precision)


NN = ((1,), (0,))
NT = ((1,), (1,))
TN = ((0,), (0,))

SCAN_HEADS_FWD = 4
SCAN_HEADS_BWD = 4


def _split_dot(m, x):
    mb = m.astype(BF16)
    x1 = x.astype(BF16)
    r1 = x - x1.astype(F32)
    x2 = r1.astype(BF16)
    x3 = (r1 - x2.astype(F32)).astype(BF16)
    return _dot(mb, x1, NN) + _dot(mb, x2, NN) + _dot(mb, x3, NN)


def _matmul(a, b, dims, out_dtype, name, tm, tn, tk, a_off=0, m_out=None):
    if dims == NN:
        m, k = a.shape[0], b.shape[0]
        n = b.shape[1]
        a_spec = pl.BlockSpec((tm, tk), lambda i, j, kk: (i, kk + a_off))
        b_spec = pl.BlockSpec((tk, tn), lambda i, j, kk: (kk, j))
    elif dims == NT:
        m, k = a.shape[0], b.shape[1]
        n = b.shape[0]
        a_spec = pl.BlockSpec((tm, tk), lambda i, j, kk: (i, kk + a_off))
        b_spec = pl.BlockSpec((tn, tk), lambda i, j, kk: (j, kk))
    else:
        m, k = (a.shape[1] if m_out is None else m_out), a.shape[0]
        n = b.shape[1]
        a_spec = pl.BlockSpec((tk, tm), lambda i, j, kk: (kk, i + a_off))
        b_spec = pl.BlockSpec((tk, tn), lambda i, j, kk: (kk, j))
    assert m % tm == 0 and n % tn == 0 and k % tk == 0, (name, m, n, k, tm, tn, tk)
    nk = k // tk

    def body(a_ref, b_ref, o_ref, *acc):
        part = _dot(a_ref[...], b_ref[...], dims)
        if nk == 1:
            o_ref[...] = part.astype(o_ref.dtype)
            return
        acc_ref, = acc
        kk = pl.program_id(2)

        @pl.when(kk == 0)
        def _():
            acc_ref[...] = part

        @pl.when(kk > 0)
        def _():
            acc_ref[...] += part

        @pl.when(kk == nk - 1)
        def _():
            o_ref[...] = acc_ref[...].astype(o_ref.dtype)

    return pl.pallas_call(
        body,
        name=name,
        grid=(m // tm, n // tn, nk),
        in_specs=[a_spec, b_spec],
        out_specs=pl.BlockSpec((tm, tn), lambda i, j, kk: (i, j)),
        out_shape=jax.ShapeDtypeStruct((m, n), out_dtype),
        scratch_shapes=[] if nk == 1 else [pltpu.VMEM((tm, tn), F32)],
        compiler_params=_cp("parallel", "parallel", "arbitrary"),
    )(a, b)


def _row(c):
    return pl.BlockSpec((TR, c), lambda i: (i, 0))


def _rowcol(width, cb):
    return pl.BlockSpec((TR, width), lambda i: (i, cb))


def _full(shape):
    return pl.BlockSpec(shape, lambda i: (0,) * len(shape))


def _mod_row(mc_ref, mx_ref, k, is_ctx):
    return jnp.where(is_ctx, mc_ref[k:k + 1, :], mx_ref[k:k + 1, :])


def _acc_row(ref, k, val):
    ref[k:k + 1, :] += val


def _acc_mod(ref, k, is_ctx, val):
    zero = jnp.zeros_like(val)
    ref[k:k + 1, :] += jnp.where(is_ctx, val, zero)
    ref[k + 1:k + 2, :] += jnp.where(is_ctx, zero, val)


def _prenorm(z, nw, modc, modx, i_shift, i_scale, name):
    t = z.shape[0]

    def body(z_ref, nw_ref, mc_ref, mx_ref, h_ref):
        is_ctx = pl.program_id(0) < NCT
        x = z_ref[...]
        n = x * _rstd(x) * nw_ref[...]
        h = n * (1.0 + _mod_row(mc_ref, mx_ref, i_scale, is_ctx)) + _mod_row(mc_ref, mx_ref, i_shift, is_ctx)
        h_ref[...] = h.astype(BF16)

    return pl.pallas_call(
        body, name=name, grid=(t // TR,),
        in_specs=[_row(D), _full((1, D)), _full((8, D)), _full((8, D))],
        out_specs=_row(D),
        out_shape=jax.ShapeDtypeStruct((t, D), BF16),
        compiler_params=_cp("parallel"),
    )(z, nw, modc, modx)


def _hg_lb(lb_ref, d):
    a0 = lb_ref[0, d:d + 1, :]
    a1 = lb_ref[1, d:d + 1, :]
    mx = jnp.maximum(a0, a1)
    e0 = jnp.exp(a0 - mx)
    e1 = jnp.exp(a1 - mx)
    return e0 / (e0 + e1)


def _log_sigmoid(x):
    return jnp.minimum(x, 0.0) - jnp.log(1.0 + jnp.exp(-jnp.abs(x)))


def _gates_fwd(p, hg_lb, wgk, bgk):
    t = p.shape[0]
    seg = lambda j: _rowcol(HW, MAIN0 // HW + j)

    def body(hq_ref, hi_ref, hf_ref, hb_ref, gq_ref, gk_ref, gv_ref, lr_ref, lb_ref, wgk_ref, bgk_ref,
             q_ref, v_ref, kf_ref, kb_ref, gf_ref, gb_ref):
        q_ref[:, :HW] = _silu(hq_ref[...].astype(F32))
        q_ref[:, HW:] = gq_ref[...].astype(F32) * (DH ** -0.5)
        v_ref[:, :HW] = hi_ref[...].astype(F32)
        v_ref[:, HW:] = gv_ref[...].astype(F32)
        xg = _dot(lr_ref[...].astype(BF16), wgk_ref[...], NN) + bgk_ref[...]
        for d, (raw_ref, k_ref, g_ref) in enumerate(((hf_ref, kf_ref, gf_ref), (hb_ref, kb_ref, gb_ref))):
            lbd = _hg_lb(lb_ref, d)
            f = lbd + (1.0 - lbd) * _sig(raw_ref[...].astype(F32))
            k_ref[:, :HW] = 1.0 - f
            k_ref[:, HW:] = gk_ref[...].astype(F32)
            g_ref[:, :HW] = jnp.log(f)
            g_ref[:, HW:] = _log_sigmoid(xg[:, d * HW:(d + 1) * HW]) * (1.0 / GLA_NORM)

    out = jax.ShapeDtypeStruct((t, D), F32)
    return pl.pallas_call(
        body, name="gates_fwd", grid=(t // TR,),
        in_specs=[seg(0), seg(1), seg(2), seg(3), seg(5), seg(6), seg(7), _rowcol(DH, LR0 // DH),
                  _full((2, 2, HW)), _full((DH, D)), _full((1, D))],
        out_specs=[_row(D)] * 6,
        out_shape=[out] * 6,
        compiler_params=_cp("parallel"),
    )(p, p, p, p, p, p, p, p, hg_lb, wgk, bgk)


def _post_fwd(o_fw, o_bw, p, onw):
    t = o_fw.shape[0]

    def body(of_ref, ob_ref, g1_ref, g2_ref, w_ref, y_ref):
        for h in range(NH):
            sl = slice(h * DH, (h + 1) * DH)
            o = of_ref[:, sl] + ob_ref[:, sl]
            g_ref = g1_ref if h < NH // 2 else g2_ref
            gs = slice((h % (NH // 2)) * DH, (h % (NH // 2) + 1) * DH)
            n = o * _rstd(o) * w_ref[:, sl]
            y_ref[:, sl] = (n * _silu(g_ref[:, gs].astype(F32))).astype(BF16)

    return pl.pallas_call(
        body, name="post_fwd", grid=(t // TR,),
        in_specs=[_row(D), _row(D), _rowcol(HW, MAIN0 // HW + 4), _rowcol(HW, MAIN0 // HW + 8), _full((1, D))],
        out_specs=_row(D),
        out_shape=jax.ShapeDtypeStruct((t, D), BF16),
        compiler_params=_cp("parallel"),
    )(o_fw, o_bw, p, p, onw)


def _merge_fwd(p, u1, u2):
    t = p.shape[0]

    def body(g1_ref, g2_ref, u1_ref, u2_ref, m_ref):
        f = lambda r: r[...].astype(F32)
        m_ref[...] = (_sig(f(g1_ref)) * f(u1_ref) + _sig(f(g2_ref)) * f(u2_ref)).astype(BF16)

    return pl.pallas_call(
        body, name="merge_fwd", grid=(t // TR,),
        in_specs=[_rowcol(D, 0), _rowcol(D, 1), _row(D), _row(D)],
        out_specs=_row(D),
        out_shape=jax.ShapeDtypeStruct((t, D), BF16),
        compiler_params=_cp("parallel"),
    )(p, p, u1, u2)


def _mid_fwd(z, y1, nw_post, nw_pre, modc, modx):
    t = z.shape[0]

    def body(z_ref, y_ref, wpo_ref, wpr_ref, mc_ref, mx_ref, z1_ref, h_ref):
        is_ctx = pl.program_id(0) < NCT
        y = y_ref[...]
        z1 = z_ref[...] + _mod_row(mc_ref, mx_ref, 2, is_ctx) * (y * _rstd(y) * wpo_ref[...])
        z1_ref[...] = z1
        n = z1 * _rstd(z1) * wpr_ref[...]
        h = n * (1.0 + _mod_row(mc_ref, mx_ref, 4, is_ctx)) + _mod_row(mc_ref, mx_ref, 3, is_ctx)
        h_ref[...] = h.astype(BF16)

    return pl.pallas_call(
        body, name="mid_fwd", grid=(t // TR,),
        in_specs=[_row(D), _row(D), _full((1, D)), _full((1, D)), _full((8, D)), _full((8, D))],
        out_specs=[_row(D), _row(D)],
        out_shape=[jax.ShapeDtypeStruct((t, D), F32), jax.ShapeDtypeStruct((t, D), BF16)],
        compiler_params=_cp("parallel"),
    )(z, y1, nw_post, nw_pre, modc, modx)


def _swiglu_fwd(uv):
    t = uv.shape[0]

    def body(u_ref, v_ref, a_ref):
        a_ref[...] = (_silu(u_ref[...].astype(F32)) * v_ref[...].astype(F32)).astype(BF16)

    return pl.pallas_call(
        body, name="swiglu_fwd", grid=(t // TR,),
        in_specs=[_rowcol(D_FF, 0), _rowcol(D_FF, 1)],
        out_specs=_row(D_FF),
        out_shape=jax.ShapeDtypeStruct((t, D_FF), BF16),
        compiler_params=_cp("parallel"),
    )(uv, uv)


def _swiglu_bwd(uv, da):
    t = uv.shape[0]

    def body(u_ref, v_ref, da_ref, d_ref):
        u = u_ref[...].astype(F32)
        d = da_ref[...].astype(F32)
        d_ref[:, :D_FF] = (d * v_ref[...].astype(F32) * _dsilu(u)).astype(BF16)
        d_ref[:, D_FF:] = (d * _silu(u)).astype(BF16)

    return pl.pallas_call(
        body, name="swiglu_bwd", grid=(t // TR,),
        in_specs=[_rowcol(D_FF, 0), _rowcol(D_FF, 1), _row(D_FF)],
        out_specs=_row(2 * D_FF),
        out_shape=jax.ShapeDtypeStruct((t, 2 * D_FF), BF16),
        compiler_params=_cp("parallel"),
    )(uv, uv, da)


def _final(z1, y2, target, nw, modc, modx):
    t = z1.shape[0]

    def body(z1_ref, y_ref, tg_ref, w_ref, mc_ref, mx_ref, dz_ref, dy_ref, loss_ref, sm_ref):
        i = pl.program_id(0)
        is_ctx = i < NCT

        @pl.when(i == 0)
        def _():
            loss_ref[...] = jnp.zeros_like(loss_ref)
            sm_ref[...] = jnp.zeros_like(sm_ref)

        g = _mod_row(mc_ref, mx_ref, 5, is_ctx)
        y = y_ref[...]
        r = _rstd(y)
        w = w_ref[...]
        yr = y * r
        n = yr * w
        e = z1_ref[...] + g * n - tg_ref[...]
        lat = jnp.where(is_ctx, 0.0, 1.0)
        loss_ref[...] += lat * _colsum(e * e)
        dz = e * (lat / D)
        dz_ref[...] = dz
        _acc_mod(sm_ref, 0, is_ctx, _colsum(dz * n))
        dn = dz * g
        _acc_row(sm_ref, 2, _colsum(dn * yr))
        dy_ref[...] = _rms_bwd(dn * w, y, r).astype(BF16)

    return pl.pallas_call(
        body, name="final", grid=(t // TR,),
        in_specs=[_row(D), _row(D), pl.BlockSpec((TR, D), lambda i: (jnp.maximum(i - NCT, 0), 0)),
                  _full((1, D)), _full((8, D)), _full((8, D))],
        out_specs=[_row(D), _row(D), _full((1, D)), _full((8, D))],
        out_shape=[jax.ShapeDtypeStruct((t, D), F32), jax.ShapeDtypeStruct((t, D), BF16),
                   jax.ShapeDtypeStruct((1, D), F32), jax.ShapeDtypeStruct((8, D), F32)],
        compiler_params=_cp("arbitrary"),
    )(z1, y2, target, nw, modc, modx)


def _mid_bwd(dh2, dz, z, z1, y1, nw_post, nw_pre, modc, modx):
    t = z.shape[0]

    def body(dh_ref, dz_ref, z_ref, z1_ref, y_ref, wpo_ref, wpr_ref, mc_ref, mx_ref, dzo_ref, dy_ref, sm_ref):
        i = pl.program_id(0)
        is_ctx = i < NCT

        @pl.when(i == 0)
        def _():
            sm_ref[...] = jnp.zeros_like(sm_ref)

        dh = dh_ref[...]
        z1 = z1_ref[...]
        r = _rstd(z1)
        zr = z1 * r
        wpr = wpr_ref[...]
        n = zr * wpr
        _acc_mod(sm_ref, 0, is_ctx, _colsum(dh))
        _acc_mod(sm_ref, 2, is_ctx, _colsum(dh * n))
        dn = dh * (1.0 + _mod_row(mc_ref, mx_ref, 4, is_ctx))
        _acc_row(sm_ref, 6, _colsum(dn * zr))
        dz1 = dz_ref[...] + _rms_bwd(dn * wpr, z1, r)
        dzo_ref[...] = dz1
        y = y_ref[...]
        r1 = _rstd(y)
        yr = y * r1
        wpo = wpo_ref[...]
        g = _mod_row(mc_ref, mx_ref, 2, is_ctx)
        _acc_mod(sm_ref, 4, is_ctx, _colsum(dz1 * (yr * wpo)))
        dn1 = dz1 * g
        _acc_row(sm_ref, 7, _colsum(dn1 * yr))
        dy_ref[...] = _rms_bwd(dn1 * wpo, y, r1).astype(BF16)

    return pl.pallas_call(
        body, name="mid_bwd", grid=(t // TR,),
        in_specs=[_row(D)] * 5 + [_full((1, D)), _full((1, D)), _full((8, D)), _full((8, D))],
        out_specs=[_row(D), _row(D), _full((8, D))],
        out_shape=[jax.ShapeDtypeStruct((t, D), F32), jax.ShapeDtypeStruct((t, D), BF16),
                   jax.ShapeDtypeStruct((8, D), F32)],
        compiler_params=_cp("arbitrary"),
    )(dh2, dz, z, z1, y1, nw_post, nw_pre, modc, modx)


def _pre_bwd(dh1, dz, z, nw, modc, modx):
    t = z.shape[0]

    def body(dh_ref, dz_ref, z_ref, w_ref, mc_ref, mx_ref, dzo_ref, sm_ref):
        i = pl.program_id(0)
        is_ctx = i < NCT

        @pl.when(i == 0)
        def _():
            sm_ref[...] = jnp.zeros_like(sm_ref)

        dh = dh_ref[...]
        x = z_ref[...]
        r = _rstd(x)
        xr = x * r
        w = w_ref[...]
        _acc_mod(sm_ref, 0, is_ctx, _colsum(dh))
        _acc_mod(sm_ref, 2, is_ctx, _colsum(dh * (xr * w)))
        dn = dh * (1.0 + _mod_row(mc_ref, mx_ref, 1, is_ctx))
        _acc_row(sm_ref, 4, _colsum(dn * xr))
        dzo_ref[...] = dz_ref[...] + _rms_bwd(dn * w, x, r)

    return pl.pallas_call(
        body, name="pre_bwd", grid=(t // TR,),
        in_specs=[_row(D)] * 3 + [_full((1, D)), _full((8, D)), _full((8, D))],
        out_specs=[pl.BlockSpec((TR, D), lambda i: (jnp.maximum(i - NCT, 0), 0)), _full((8, D))],
        out_shape=[jax.ShapeDtypeStruct((t - CTX, D), F32), jax.ShapeDtypeStruct((8, D), F32)],
        compiler_params=_cp("arbitrary"),
    )(dh1, dz, z, nw, modc, modx)


def _merge_bwd(dm, p, u1, u2):
    t = dm.shape[0]

    def body(dm_ref, g1_ref, g2_ref, u1_ref, u2_ref, du1_ref, du2_ref, dg_ref):
        dm_ = dm_ref[...]
        s1 = _sig(g1_ref[...].astype(F32))
        s2 = _sig(g2_ref[...].astype(F32))
        du1_ref[...] = (dm_ * s1).astype(BF16)
        du2_ref[...] = (dm_ * s2).astype(BF16)
        dg_ref[:, :D] = (dm_ * u1_ref[...].astype(F32) * s1 * (1.0 - s1)).astype(BF16)
        dg_ref[:, D:] = (dm_ * u2_ref[...].astype(F32) * s2 * (1.0 - s2)).astype(BF16)

    return pl.pallas_call(
        body, name="merge_bwd", grid=(t // TR,),
        in_specs=[_row(D), _rowcol(D, 0), _rowcol(D, 1), _row(D), _row(D)],
        out_specs=[_row(D), _row(D), _row(2 * D)],
        out_shape=[jax.ShapeDtypeStruct((t, D), BF16), jax.ShapeDtypeStruct((t, D), BF16),
                   jax.ShapeDtypeStruct((t, 2 * D), BF16)],
        compiler_params=_cp("parallel"),
    )(dm, p, p, u1, u2)


def _post_bwd(dy_hg, dy_gla, o_fw, o_bw, p, onw):
    t = o_fw.shape[0]

    def body(d1_ref, d2_ref, of_ref, ob_ref, g1_ref, g2_ref, w_ref, do_ref, dg_ref, sm_ref):
        @pl.when(pl.program_id(0) == 0)
        def _():
            sm_ref[...] = jnp.zeros_like(sm_ref)

        for h in range(NH):
            sl = slice(h * DH, (h + 1) * DH)
            gs = slice((h % (NH // 2)) * DH, (h % (NH // 2) + 1) * DH)
            g_ref, d_ref = (g1_ref, d1_ref) if h < NH // 2 else (g2_ref, d2_ref)
            o = of_ref[:, sl] + ob_ref[:, sl]
            r = _rstd(o)
            orr = o * r
            w = w_ref[:, sl]
            gt = g_ref[:, gs].astype(F32)
            dy = d_ref[:, gs]
            dg_ref[:, sl] = (dy * (orr * w) * _dsilu(gt)).astype(BF16)
            dn = dy * _silu(gt)
            sm_ref[0:1, sl] += _colsum(dn * orr)
            do_ref[:, sl] = _rms_bwd(dn * w, o, r)

    return pl.pallas_call(
        body, name="post_bwd", grid=(t // TR,),
        in_specs=[_row(HW), _row(HW), _row(D), _row(D), _rowcol(HW, MAIN0 // HW + 4), _rowcol(HW, MAIN0 // HW + 8),
                  _full((1, D))],
        out_specs=[_row(D), _row(D), _full((8, D))],
        out_shape=[jax.ShapeDtypeStruct((t, D), F32), jax.ShapeDtypeStruct((t, D), BF16),
                   jax.ShapeDtypeStruct((8, D), F32)],
        compiler_params=_cp("arbitrary"),
    )(dy_hg, dy_gla, o_fw, o_bw, p, p, onw)


def _gates_bwd(p, hg_lb, wgk, bgk, dgm, dgo, dq_f, dq_b, dv_f, dv_b, dk_f, dk_b, dg_f, dg_b):
    t = p.shape[0]
    seg = lambda j: _rowcol(HW, MAIN0 // HW + j)

    def body(hq_ref, hf_ref, hb_ref, lr_ref, lb_ref, wgk_ref, bgk_ref, dgm_ref, dgo_ref,
             dqf_ref, dqb_ref, dvf_ref, dvb_ref, dkf_ref, dkb_ref, dgf_ref, dgb_ref,
             dp_ref, dlb_ref, dw_ref, db_ref):
        @pl.when(pl.program_id(0) == 0)
        def _():
            dlb_ref[...] = jnp.zeros_like(dlb_ref)
            dw_ref[...] = jnp.zeros_like(dw_ref)
            db_ref[...] = jnp.zeros_like(db_ref)

        c0 = MAIN0

        def put(j, val):
            dp_ref[:, c0 + j * HW:c0 + (j + 1) * HW] = val.astype(BF16)

        dp_ref[:, :MAIN0] = dgm_ref[...]
        dq = dqf_ref[...] + dqb_ref[...]
        dv = dvf_ref[...] + dvb_ref[...]
        put(0, dq[:, :HW] * _dsilu(hq_ref[...].astype(F32)))
        put(1, dv[:, :HW])
        put(5, dq[:, HW:] * (DH ** -0.5))
        put(7, dv[:, HW:])
        put(6, dkf_ref[:, HW:] + dkb_ref[:, HW:])
        dp_ref[:, c0 + 4 * HW:c0 + 5 * HW] = dgo_ref[:, :HW]
        dp_ref[:, c0 + 8 * HW:c0 + 9 * HW] = dgo_ref[:, HW:]
        lr = lr_ref[...].astype(BF16)
        xg = _dot(lr, wgk_ref[...], NN) + bgk_ref[...]
        dxg = []
        for d, (raw_ref, dk_ref, dg_ref) in enumerate(((hf_ref, dkf_ref, dgf_ref), (hb_ref, dkb_ref, dgb_ref))):
            lbd = _hg_lb(lb_ref, d)
            s = _sig(raw_ref[...].astype(F32))
            f = lbd + (1.0 - lbd) * s
            df = dg_ref[:, :HW] / f - dk_ref[:, :HW]
            put(2 + d, df * (1.0 - lbd) * s * (1.0 - s))
            dlb_ref[d:d + 1, :] += _colsum(df * (1.0 - s)) * (lbd * (1.0 - lbd))
            dxg.append(dg_ref[:, HW:] * (1.0 / GLA_NORM) * _sig(-xg[:, d * HW:(d + 1) * HW]))
        dxg = jnp.concatenate(dxg, axis=1)
        db_ref[0:1, :] += _colsum(dxg)
        dxg_b = dxg.astype(BF16)
        dw_ref[...] += _dot(lr, dxg_b, TN)
        dp_ref[:, LR0:LR0 + DH] = _dot(dxg_b, wgk_ref[...], NT).astype(BF16)
        dp_ref[:, LR0 + DH:] = jnp.zeros((TR, W_IN_COLS - LR0 - DH), BF16)

    return pl.pallas_call(
        body, name="gates_bwd", grid=(t // TR,),
        in_specs=[seg(0), seg(2), seg(3), _rowcol(DH, LR0 // DH), _full((2, 2, HW)), _full((DH, D)), _full((1, D)),
                  _row(2 * D), _row(D)] + [_row(D)] * 8,
        out_specs=[_row(W_IN_COLS), _full((8, HW)), _full((DH, D)), _full((8, D))],
        out_shape=[jax.ShapeDtypeStruct((t, W_IN_COLS), BF16), jax.ShapeDtypeStruct((8, HW), F32),
                   jax.ShapeDtypeStruct((DH, D), F32), jax.ShapeDtypeStruct((8, D), F32)],
        compiler_params=_cp("arbitrary"),
    )(p, p, p, p, hg_lb, wgk, bgk, dgm, dgo, dq_f, dq_b, dv_f, dv_b, dk_f, dk_b, dg_f, dg_b)


def _scan_consts(rev):
    r = lax.broadcasted_iota(jnp.int32, (CHUNK, CHUNK), 0)
    u = lax.broadcasted_iota(jnp.int32, (CHUNK, CHUNK), 1)
    rp = lax.broadcasted_iota(jnp.int32, (CHUNK, 1), 0)
    if rev:
        r, u, rp = CHUNK - 1 - r, CHUNK - 1 - u, CHUNK - 1 - rp
    tri = jnp.where(u <= r, 1.0, 0.0).astype(F32)
    tri_t = jnp.where(r <= u, 1.0, 0.0).astype(F32)
    lv = []
    for b in LEVELS:
        sh = b.bit_length() - 1
        pair = ((r >> sh) == (u >> sh) + 1) & (((u >> sh) & 1) == 0)
        pair_t = ((u >> sh) == (r >> sh) + 1) & (((r >> sh) & 1) == 0)
        tside = ((rp >> sh) & 1) == 1
        lv.append((pair, pair_t, tside))
    bd = LEVELS[-1].bit_length() - 1
    diag = ((r >> bd) == (u >> bd)) & (u <= r)
    diag_t = ((r >> bd) == (u >> bd)) & (r <= u)
    return tri, tri_t, lv, diag, diag_t


def _row_of(pos, rev):
    return CHUNK - 1 - pos if rev else pos


def _chunk_terms(cum, b_scr, consts, rev):
    _, _, lv, _, _ = consts
    terms = []
    for b, (_, _, tside) in zip(LEVELS, lv):
        pieces = []
        for j in range(CHUNK // (2 * b)):
            row = _row_of(2 * b * j + b - 1, rev)
            pieces.append(jnp.broadcast_to(b_scr[row:row + 1, :], (2 * b, DH)))
        if rev:
            pieces = pieces[::-1]
        bnd = pieces[0] if len(pieces) == 1 else jnp.concatenate(pieces, axis=0)
        w = jnp.exp(jnp.minimum(jnp.where(tside, cum - bnd, bnd - cum), 0.0))
        wq = jnp.where(tside, w, 0.0)
        wk = jnp.where(tside, 0.0, w)
        terms.append((wq, wk))
    b = LEVELS[-1]
    pieces = []
    for j in range(CHUNK // b):
        if j == 0:
            pieces.append(jnp.zeros((b, DH), F32))
        else:
            row = _row_of(b * j - 1, rev)
            pieces.append(jnp.broadcast_to(b_scr[row:row + 1, :], (b, DH)))
    if rev:
        pieces = pieces[::-1]
    start = jnp.concatenate(pieces, axis=0)
    wq = jnp.exp(jnp.minimum(cum - start, 0.0))
    wk = jnp.exp(jnp.minimum(start - cum, EXP_CLAMP))
    terms.append((wq, wk))
    return terms


def _run_staged(units):
    live = list(units)
    while live:
        nxt = []
        for u in live:
            try:
                next(u)
                nxt.append(u)
            except StopIteration:
                pass
        live = nxt


SCAN_TB = 256
SCAN_CB = SCAN_TB // CHUNK


def _block_order(i, ntb, rev):
    nctx = CTX // SCAN_TB
    if not rev:
        return i
    return jnp.where(i < nctx, nctx - 1 - i, ntb - 1 - (i - nctx))


def _chunk_in_block(j, rev):
    return SCAN_CB - 1 - j if rev else j


def _scan_fwd(q, k, v, g, rev):
    t = q.shape[0]
    nc = t // CHUNK
    hpb = SCAN_HEADS_FWD

    def body(q_ref, k_ref, v_ref, g_ref, o_ref, st_ref, s_scr, b_scr):
        consts = _scan_consts(rev)
        _, _, lv, diag, _ = consts
        masks = [pair for pair, _, _ in lv] + [diag]

        @pl.when(pl.program_id(1) == 0)
        def _():
            s_scr[...] = jnp.zeros_like(s_scr)

        tri = consts[0]
        state = {hh: s_scr[hh] for hh in range(hpb)}

        def unit(hh, j):
            sl = slice(hh * DH, (hh + 1) * DH)
            c = _chunk_in_block(j, rev)
            rows = slice(c * CHUNK, (c + 1) * CHUNK)
            b_ref = b_scr.at[hh * SCAN_CB + j]
            qc, kc, vc, gc = q_ref[rows, sl], k_ref[rows, sl], v_ref[rows, sl], g_ref[rows, sl]
            cum = _split_dot(tri, gc)
            b_ref[...] = cum
            yield
            terms = _chunk_terms(cum, b_ref, consts, rev)
            ops = [((qc * wq).astype(BF16), (kc * wk).astype(BF16)) for wq, wk in terms]
            tot = _colsum(gc)
            qe = (qc * jnp.exp(cum)).astype(BF16)
            ke = (kc * jnp.exp(tot - cum)).astype(BF16)
            vb = vc.astype(BF16)
            yield
            scs = [_dot(qt, kt, NT) for qt, kt in ops]
            kv = _dot(vb, ke, TN)
            yield
            a = jnp.zeros((CHUNK, CHUNK), F32)
            for sc, m in zip(scs, masks):
                a = a + jnp.where(m, sc, 0.0)
            o_intra = _dot(a.astype(BF16), vb, NN)
            yield
            st = state[hh]
            st_ref[hh, c] = st
            o_ref[rows, sl] = o_intra + _dot(qe, st.astype(BF16), NT)
            state[hh] = st * jnp.exp(tot) + kv
            yield

        _run_staged([unit(hh, j) for hh in range(hpb) for j in range(SCAN_CB)])
        for hh in range(hpb):
            s_scr[hh] = state[hh]

    ntb = t // SCAN_TB
    col = pl.BlockSpec((SCAN_TB, hpb * DH), lambda h, i: (_block_order(i, ntb, rev), h))
    return pl.pallas_call(
        body, name="scan_fwd_" + ("bw" if rev else "fw"), grid=(NH // hpb, ntb),
        in_specs=[col] * 4,
        out_specs=[col, pl.BlockSpec((hpb, SCAN_CB, DH, DH), lambda h, i: (h, _block_order(i, ntb, rev), 0, 0))],
        out_shape=[jax.ShapeDtypeStruct((t, D), F32), jax.ShapeDtypeStruct((NH, nc, DH, DH), F32)],
        scratch_shapes=[pltpu.VMEM((hpb, DH, DH), F32), pltpu.VMEM((hpb * SCAN_CB, CHUNK, DH), F32)],
        compiler_params=_cp("parallel", "arbitrary"),
    )(q, k, v, g)


def _scan_bwd(q, k, v, g, do, states, rev):
    t = q.shape[0]
    nc = t // CHUNK
    hpb = SCAN_HEADS_BWD

    def body(q_ref, k_ref, v_ref, g_ref, do_ref, st_ref, dq_ref, dk_ref, dv_ref, dg_ref, ds_scr, b_scr):
        consts = _scan_consts(rev)
        _, tri_t, lv, diag, diag_t = consts
        masks = [(pair, pair_t) for pair, pair_t, _ in lv] + [(diag, diag_t)]
        @pl.when(pl.program_id(1) == 0)
        def _():
            ds_scr[...] = jnp.zeros_like(ds_scr)

        tri = consts[0]
        dstate = {hh: ds_scr[hh] for hh in range(hpb)}

        def unit(hh, jj):
            sl = slice(hh * DH, (hh + 1) * DH)
            c = _chunk_in_block(SCAN_CB - 1 - jj, rev)
            rows = slice(c * CHUNK, (c + 1) * CHUNK)
            b_ref = b_scr.at[hh * SCAN_CB + jj]
            qc, kc, vc, gc = q_ref[rows, sl], k_ref[rows, sl], v_ref[rows, sl], g_ref[rows, sl]
            dob = do_ref[rows, sl].astype(BF16)
            vb = vc.astype(BF16)
            cum = _split_dot(tri, gc)
            b_ref[...] = cum
            da = _dot(dob, vb, NT)
            da_t = _dot(vb, dob, NT)
            yield
            terms = _chunk_terms(cum, b_ref, consts, rev)
            ops = [((qc * wq).astype(BF16), (kc * wk).astype(BF16)) for wq, wk in terms]
            tot = _colsum(gc)
            e_tot = jnp.exp(tot)
            e_b = jnp.exp(cum)
            e_t = jnp.exp(tot - cum)
            qeb = (qc * e_b).astype(BF16)
            keb = (kc * e_t).astype(BF16)
            dal = [(jnp.where(m, da, 0.0).astype(BF16), jnp.where(m_t, da_t, 0.0).astype(BF16)) for m, m_t in masks]
            yield
            ats = [_dot(ktb, qtb, NT) for qtb, ktb in ops]
            dqts = [_dot(d, ktb, NN) for (d, _), (_, ktb) in zip(dal, ops)]
            dkts = [_dot(d_t, qtb, NN) for (_, d_t), (qtb, _) in zip(dal, ops)]
            qd = _dot(dob, qeb, TN)
            yield
            a_t = jnp.zeros((CHUNK, CHUNK), F32)
            dq = jnp.zeros((CHUNK, DH), F32)
            dk = jnp.zeros((CHUNK, DH), F32)
            db = jnp.zeros((CHUNK, DH), F32)
            for at, dqt, dkt, (wq, wk), (qtb, ktb), (_, m_t) in zip(ats, dqts, dkts, terms, ops, masks):
                a_t = a_t + jnp.where(m_t, at, 0.0)
                dq = dq + dqt * wq
                dk = dk + dkt * wk
                db = db + dqt * qtb.astype(F32) - dkt * ktb.astype(F32)
            dv_intra = _dot(a_t.astype(BF16), dob, NN)
            st = st_ref[hh, c]
            stb = st.astype(BF16)
            dqe = _dot(dob, stb, NN)
            yield
            dst = dstate[hh]
            dstb = dst.astype(BF16)
            dstate[hh] = dst * e_tot + qd
            dv_ref[rows, sl] = dv_intra + _dot(keb, dstb, NT)
            dke = _dot(vb, dstb, NN)
            yield
            qe = qeb.astype(F32)
            ke = keb.astype(F32)
            dq_ref[rows, sl] = dq + dqe * e_b
            dk_ref[rows, sl] = dk + dke * e_t
            db = db + dqe * qe - dke * ke
            dtot = _colsum(dstb.astype(F32) * stb.astype(F32)) * e_tot + _colsum(dke * ke)
            dg_ref[rows, sl] = _split_dot(tri_t, db) + dtot
            yield

        _run_staged([unit(hh, jj) for hh in range(hpb) for jj in range(SCAN_CB)])
        for hh in range(hpb):
            ds_scr[hh] = dstate[hh]

    ntb = t // SCAN_TB
    blk = lambda i: _block_order(ntb - 1 - i, ntb, rev)
    col = pl.BlockSpec((SCAN_TB, hpb * DH), lambda h, i: (blk(i), h))
    out = jax.ShapeDtypeStruct((t, D), F32)
    return pl.pallas_call(
        body, name="scan_bwd_" + ("bw" if rev else "fw"), grid=(NH // hpb, ntb),
        in_specs=[col] * 5 + [pl.BlockSpec((hpb, SCAN_CB, DH, DH), lambda h, i: (h, blk(i), 0, 0))],
        out_specs=[col] * 4,
        out_shape=[out] * 4,
        scratch_shapes=[pltpu.VMEM((hpb, DH, DH), F32), pltpu.VMEM((hpb * SCAN_CB, CHUNK, DH), F32)],
        compiler_params=_cp("parallel", "arbitrary"),
    )(q, k, v, g, do, states)


W_IN_GRAD_ROWS = ((0, 352), (352, 688), (688, 1024))
W_IN_REF = 6688
GATE0 = 4640
LRW = 32


def _layout_w_in(w):
    return jnp.concatenate([w[:, GATE0:], w[:, :GATE0 - LRW], w[:, GATE0 - LRW:GATE0],
                            jnp.zeros((w.shape[0], W_IN_COLS - W_IN_REF), w.dtype)], axis=1)


def _unlayout_w_in(d):
    return jnp.concatenate([d[:, MAIN0:LR0 + LRW], d[:, :MAIN0]], axis=1)


def _layout_wgk(w):
    r = w.shape[1]
    top = jnp.concatenate([w[0], jnp.zeros_like(w[0])], axis=1)
    bot = jnp.concatenate([jnp.zeros_like(w[1]), w[1]], axis=1)
    return jnp.concatenate([top, bot, jnp.zeros((DH - 2 * r, D), w.dtype)], axis=0)


def _unlayout_wgk(d, r=16):
    return jnp.stack([d[:r, :HW], d[r:2 * r, HW:]])


def _local_step(z, target, modc, modx, norms, onw, hg_lb, wgk, bgk, w_in, get_mix, get_ffn, send):
    n_pre1, n_post1, n_pre2, n_post2 = norms
    t = z.shape[0]
    tm = 768 if t % 768 == 0 else 256
    h1 = _prenorm(z, n_pre1, modc, modx, 0, 1, "prenorm1")
    p = _matmul(h1, w_in, NN, BF16, "mm_in", tm, 512, D)
    q, v, k_f, k_b, g_f, g_b = _gates_fwd(p, hg_lb, wgk, bgk)
    o_f, st_f = _scan_fwd(q, k_f, v, g_f, False)
    o_b, st_b = _scan_fwd(q, k_b, v, g_b, True)
    y = _post_fwd(o_f, o_b, p, onw)
    w_br_hg, w_br_gla, w_out = get_mix(y)
    u1 = _matmul(y, w_br_hg, NN, BF16, "mm_br_hg", tm, 512, HW, a_off=0)
    u2 = _matmul(y, w_br_gla, NN, BF16, "mm_br_gla", tm, 512, HW, a_off=1)
    merged = _merge_fwd(p, u1, u2)
    y1 = _matmul(merged, w_out, NN, F32, "mm_out", tm, 512, D)
    z1, h2 = _mid_fwd(z, y1, n_post1, n_pre2, modc, modx)
    w_gu_t, w_down = get_ffn(h2)
    uv = _matmul(h2, w_gu_t, NT, BF16, "mm_gu", tm, 512, D)
    act = _swiglu_fwd(uv)
    y2 = _matmul(act, w_down, NN, F32, "mm_down", tm, 512, D_FF // 2)
    dz, dy2, loss_vec, sm_final = _final(z1, y2, target, n_post2, modc, modx)
    dact = _matmul(dy2, w_down, NT, BF16, "mm_down_dx", tm, D_FF // 2, D)
    d_w_down = _matmul(act, dy2, TN, BF16, "mm_down_dw", D_FF // 2, 512, t)
    duv = _swiglu_bwd(uv, dact)
    dh2 = _matmul(duv, w_gu_t, NN, F32, "mm_gu_dx", tm, 512, D_FF // 2)
    d_w_gu_t = _matmul(duv, h2, TN, BF16, "mm_gu_dw", 512, 512, t)
    dh2 = send(("w_down", "w_gu_t"), (d_w_down, d_w_gu_t), dh2)
    dz, dy1, sm_mid = _mid_bwd(dh2, dz, z, z1, y1, n_post1, n_pre2, modc, modx)
    dmerged = _matmul(dy1, w_out, NT, F32, "mm_out_dx", tm, 512, D)
    d_w_out = _matmul(merged, dy1, TN, BF16, "mm_out_dw", 512, 512, t)
    du1, du2, dgm = _merge_bwd(dmerged, p, u1, u2)
    dy_hg = _matmul(du1, w_br_hg, NT, F32, "mm_br_hg_dx", tm, HW, D)
    dy_gla = _matmul(du2, w_br_gla, NT, F32, "mm_br_gla_dx", tm, HW, D)
    d_w_br_hg = _matmul(y, du1, TN, BF16, "mm_br_hg_dw", HW, 512, t, a_off=0, m_out=HW)
    d_w_br_gla = _matmul(y, du2, TN, BF16, "mm_br_gla_dw", HW, 512, t, a_off=1, m_out=HW)
    dy_hg = send(("w_out", "w_br_hg", "w_br_gla"), (d_w_out, d_w_br_hg, d_w_br_gla), dy_hg)
    do, dgo, sm_post = _post_bwd(dy_hg, dy_gla, o_f, o_b, p, onw)
    dq_f, dk_f, dv_f, dg_f = _scan_bwd(q, k_f, v, g_f, do, st_f, False)
    dq_b, dk_b, dv_b, dg_b = _scan_bwd(q, k_b, v, g_b, do, st_b, True)
    dp, d_lb, d_wgk, d_bgk = _gates_bwd(p, hg_lb, wgk, bgk, dgm, dgo, dq_f, dq_b, dv_f, dv_b, dk_f, dk_b, dg_f, dg_b)
    d_w_in = _matmul(h1, dp, TN, BF16, "mm_in_dw", 512, 512, t)
    dp = send(("w_in",), (d_w_in,), dp)
    dh1 = _matmul(dp, w_in, NT, F32, "mm_in_dx", tm, 512, 1024)
    grad_x, sm_pre = _pre_bwd(dh1, dz, z, n_pre1, modc, modx)
    return dict(loss_vec=loss_vec, grad_x=grad_x, sm_final=sm_final, sm_mid=sm_mid, sm_post=sm_post, sm_pre=sm_pre,
                d_lb=d_lb, d_wgk=d_wgk, d_bgk=d_bgk)


MESH = pl.DeviceIdType.MESH
ANY = pl.BlockSpec(memory_space=pl.ANY)
N_REL = N_DEV - 1


def _place():
    return lax.axis_index("x"), lax.axis_index("y"), lax.axis_index("c")


def _slot(p):
    return 4 * p[0] + 2 * p[1] + p[2]


def _all_gather(arrays, name):
    n = len(arrays)

    def body(*refs):
        ins, outs = refs[:n], refs[n:2 * n]
        send_sems, recv_sems, local_sems = refs[2 * n:]
        x, y, c = _place()
        me, sibling = (x, y, c), (x, y, 1 - c)
        chips = [(1 - x, y), (x, 1 - y), (1 - x, 1 - y)]

        def copy(a, k, block, to, src=None):
            dst = outs[a].at[_slot(block)]
            return pltpu.make_async_remote_copy(
                src_ref=dst if src is None else src, dst_ref=dst,
                send_sem=send_sems.at[N_REL * a + k], recv_sem=recv_sems.at[N_REL * a + k],
                device_id=to, device_id_type=MESH)

        mine = [pltpu.make_async_copy(ins[a], outs[a].at[_slot(me)], local_sems.at[a]) for a in range(n)]
        for cp in mine:
            cp.start()
        first = []
        for a in range(n):
            first.append(copy(a, 0, me, sibling, src=ins[a]))
            first += [copy(a, 1 + j, me, (*chip, c), src=ins[a]) for j, chip in enumerate(chips)]
        for cp in first:
            cp.start()
        passed = []
        for j, chip in enumerate(chips):
            for a in range(n):
                copy(a, 1 + j, (*chip, c), me).wait_recv()
                fwd = copy(a, 4 + j, (*chip, c), sibling)
                fwd.start()
                passed.append(fwd)
        for a in range(n):
            copy(a, 0, sibling, me).wait_recv()
        for j, chip in enumerate(chips):
            for a in range(n):
                copy(a, 4 + j, (*chip, 1 - c), me).wait_recv()
        for cp in first + passed:
            cp.wait_send()
        for cp in mine:
            cp.wait()

    return pl.pallas_call(
        body, name=name,
        in_specs=[ANY] * n, out_specs=[ANY] * n,
        out_shape=[jax.ShapeDtypeStruct((N_DEV,) + a.shape, a.dtype) for a in arrays],
        scratch_shapes=[pltpu.SemaphoreType.DMA((N_REL * n,)), pltpu.SemaphoreType.DMA((N_REL * n,)),
                        pltpu.SemaphoreType.DMA((n,))],
    )(*arrays)


def _exchange(arrays, name):
    n = len(arrays)

    def body(*refs):
        ins, outs = refs[:n], refs[n:2 * n]
        send_sems, recv_sems, local_sems = refs[2 * n:]
        x, y, c = _place()
        me = _slot((x, y, c))
        mine = [pltpu.make_async_copy(ins[a].at[me], outs[a].at[me], local_sems.at[a]) for a in range(n)]
        for cp in mine:
            cp.start()
        copies = []
        for a in range(n):
            for k in range(1, N_DEV):
                flip = lambda v, bit: 1 - v if bit else v
                peer = (flip(x, k & 4), flip(y, k & 2), flip(c, k & 1))
                copies.append(pltpu.make_async_remote_copy(
                    src_ref=ins[a].at[_slot(peer)], dst_ref=outs[a].at[me],
                    send_sem=send_sems.at[N_REL * a + k - 1], recv_sem=recv_sems.at[N_REL * a + k - 1],
                    device_id=peer, device_id_type=MESH))
                copies[-1].start()
        i = 0
        for a in range(n):
            for k in range(1, N_DEV):
                flip = lambda v, bit: 1 - v if bit else v
                peer = (flip(x, k & 4), flip(y, k & 2), flip(c, k & 1))
                pltpu.make_async_remote_copy(
                    src_ref=ins[a].at[_slot(peer)], dst_ref=outs[a].at[_slot(peer)],
                    send_sem=send_sems.at[N_REL * a + k - 1], recv_sem=recv_sems.at[N_REL * a + k - 1],
                    device_id=peer, device_id_type=MESH).wait_recv()
                i += 1
        for cp in copies:
            cp.wait_send()
        for cp in mine:
            cp.wait()

    return pl.pallas_call(
        body, name=name,
        in_specs=[ANY] * n, out_specs=[ANY] * n,
        out_shape=[jax.ShapeDtypeStruct(a.shape, a.dtype) for a in arrays],
        scratch_shapes=[pltpu.SemaphoreType.DMA((N_REL * n,)), pltpu.SemaphoreType.DMA((N_REL * n,)),
                        pltpu.SemaphoreType.DMA((n,))],
    )(*arrays)


HBM = pl.BlockSpec(memory_space=pltpu.HBM)
SEM = pl.BlockSpec(memory_space=pltpu.SEMAPHORE)
EFFECT = pltpu.SideEffectType.DATAFLOW_SIDE_EFFECTING


def _peer_of(x, y, c, k):
    flip = lambda v, bit: 1 - v if bit else v
    return flip(x, k & 4), flip(y, k & 2), flip(c, k & 1)


def _split_copies(gather, srcs, lands, send_sems, recv_sems, local_sems):
    x, y, c = _place()
    me = _slot((x, y, c))
    local, sends, waits = [], [], []
    for a, (src, land) in enumerate(zip(srcs, lands)):
        local.append(pltpu.make_async_copy(src if gather else src.at[me], land.at[me], local_sems.at[a]))
        for k in range(1, N_DEV):
            peer = _peer_of(x, y, c, k)
            mine = src if gather else src.at[_slot(peer)]
            sems = dict(send_sem=send_sems.at[N_REL * a + k - 1], recv_sem=recv_sems.at[N_REL * a + k - 1],
                        device_id=peer, device_id_type=MESH)
            sends.append(pltpu.make_async_remote_copy(src_ref=mine, dst_ref=land.at[me], **sems))
            waits.append(pltpu.make_async_remote_copy(src_ref=mine, dst_ref=land.at[_slot(peer)], **sems))
    return local, sends, waits


def _split_start(gather, srcs, name, after):
    n = len(srcs)
    land_shapes = [((N_DEV,) + s.shape) if gather else s.shape for s in srcs]
    lands = [lax.empty(shp, s.dtype) for shp, s in zip(land_shapes, srcs)]

    def body(*refs):
        src_refs, land_refs = refs[:n], refs[n:2 * n]
        send_sems, recv_sems, local_sems = refs[2 * n + 1:2 * n + 4]
        token = refs[-1]
        local, sends, _ = _split_copies(gather, src_refs, land_refs, send_sems, recv_sems, local_sems)
        for cp in local + sends:
            cp.start()
        token[...] = jnp.zeros_like(token)

    hbm = lambda a: pltpu.with_memory_space_constraint(a, pltpu.HBM)
    out = pl.pallas_call(
        body, name=name,
        out_shape=(pltpu.SemaphoreType.DMA((N_REL * n,)), pltpu.SemaphoreType.DMA((N_REL * n,)),
                   pltpu.SemaphoreType.DMA((n,)),
                   *[pltpu.HBM(s.shape, s.dtype) for s in srcs], *[pltpu.HBM(l.shape, l.dtype) for l in lands],
                   jax.ShapeDtypeStruct((8, DH), F32)),
        in_specs=[HBM] * (2 * n) + [ANY],
        out_specs=(SEM, SEM, SEM, *([HBM] * (2 * n)), pl.BlockSpec(memory_space=pltpu.VMEM)),
        input_output_aliases={i: 3 + i for i in range(2 * n)},
        compiler_params=pltpu.CompilerParams(has_side_effects=EFFECT),
    )(*[hbm(s) for s in srcs], *[hbm(l) for l in lands], after)
    return (gather, n, out[:3], out[3:3 + n], out[3 + n:3 + 2 * n]), out[-1]


def _split_wait(handle, name, after):
    gather, n, sems, srcs, lands = handle

    def body(*refs):
        src_refs, land_refs = refs[:n], refs[n:2 * n]
        send_sems, recv_sems, local_sems = refs[2 * n:2 * n + 3]
        local, _, waits = _split_copies(gather, src_refs, land_refs, send_sems, recv_sems, local_sems)
        for cp in waits:
            cp.wait_send()
            cp.wait_recv()
        for cp in local:
            cp.wait()

    out = pl.pallas_call(
        body, name=name,
        out_shape=(*[pltpu.HBM(s.shape, s.dtype) for s in srcs], *[pltpu.HBM(l.shape, l.dtype) for l in lands]),
        in_specs=[HBM] * (2 * n) + [SEM, SEM, SEM, ANY],
        out_specs=tuple([HBM] * (2 * n)),
        input_output_aliases={i: i for i in range(2 * n)},
        compiler_params=pltpu.CompilerParams(has_side_effects=EFFECT),
    )(*srcs, *lands, *sems, after)
    return list(out[n:])


def _tie(x, token, name):
    def body(x_ref, t_ref, o_ref):
        pass

    return pl.pallas_call(
        body, name=name, out_shape=jax.ShapeDtypeStruct(x.shape, x.dtype),
        in_specs=[ANY, ANY], out_specs=ANY, input_output_aliases={0: 0},
    )(x, token)


def _mod_fwd(a, w, b):
    def body(a_ref, w_ref, b_ref, o_ref):
        o_ref[...] = _dot(_silu(a_ref[...]), w_ref[...], NN, precision=HI) + b_ref[...]

    return pl.pallas_call(
        body, name="mod_fwd", out_shape=jax.ShapeDtypeStruct((a.shape[0], w.shape[1]), F32),
        compiler_params=pltpu.CompilerParams(vmem_limit_bytes=VMEM_LIMIT),
    )(a, w, b)


def _mod_bwd(a, d, w):
    def body(a_ref, d_ref, w_ref, dw_ref, dc_ref):
        av = a_ref[...]
        dv = d_ref[...]
        dw_ref[...] = _dot(_silu(av), dv, TN, precision=HI)
        da = _dot(dv[0:8, :], w_ref[...], NT, precision=HI) * _dsilu(av[0:8, :])
        row = lax.broadcasted_iota(jnp.int32, da.shape, 0)
        dc_ref[...] = jnp.where(row == 0, da, 0.0)

    return pl.pallas_call(
        body, name="mod_bwd",
        out_shape=[jax.ShapeDtypeStruct(w.shape, F32), jax.ShapeDtypeStruct((8, w.shape[0]), F32)],
        compiler_params=pltpu.CompilerParams(vmem_limit_bytes=VMEM_LIMIT),
    )(a, d, w)


def _sum_devices(g):
    def body(g_ref, o_ref):
        acc = g_ref[0]
        for i in range(1, g.shape[0]):
            acc = acc + g_ref[i]
        o_ref[...] = acc

    return pl.pallas_call(body, name="sum_devices_%d" % g.shape[1],
                          out_shape=jax.ShapeDtypeStruct(g.shape[1:], F32))(g)


def _adam_rows(r, c, n):
    budget = 6 * 1024 * 1024
    best = None
    for tr in range(16, r + 1, 16):
        if r % tr == 0 and tr * c * (2 * n + 28) <= budget:
            best = tr
    return best if best is not None else r


def _adamw(g, w, m, v, name):
    n, r, c = g.shape
    tr = _adam_rows(r, c, n)
    bc1 = 1.0 - ADAM_B1 ** ADAM_STEP
    bc2 = 1.0 - ADAM_B2 ** ADAM_STEP

    def body(g_ref, w_ref, m_ref, v_ref, go_ref, d_ref, mo_ref, vo_ref):
        grad = g_ref[0].astype(F32)
        for i in range(1, n):
            grad = grad + g_ref[i].astype(F32)
        go_ref[...] = grad
        m_new = ADAM_B1 * m_ref[...] + (1.0 - ADAM_B1) * grad
        v_new = ADAM_B2 * v_ref[...] + (1.0 - ADAM_B2) * (grad * grad)
        mo_ref[...] = m_new
        vo_ref[...] = v_new
        d_ref[...] = -ADAM_LR * ((m_new / bc1) / (jnp.sqrt(v_new / bc2) + ADAM_EPS) + ADAM_WD * w_ref[...])

    blk = pl.BlockSpec((tr, c), lambda i: (i, 0))
    out = jax.ShapeDtypeStruct((r, c), F32)
    return pl.pallas_call(
        body, name=name, grid=(r // tr,),
        in_specs=[pl.BlockSpec((n, tr, c), lambda i: (0, i, 0)), blk, blk, blk],
        out_specs=[blk] * 4, out_shape=[out] * 4,
        compiler_params=_cp("parallel"),
    )(g, w, m, v)


def kernel(x, c, ctx, c_ctx, w_mod, b_mod, norm_pre1, norm_post1, norm_pre2, norm_post2, w_in, hg_lb, hg_onorm, gla_w_gk, gla_b_gk, gla_onorm, w_br_hg, w_br_gla, w_out, w_ff_gate, w_ff_up, w_ff_down, loss_target, m_c_ctx, m_w_mod, m_b_mod, m_norm_pre1, m_norm_post1, m_norm_pre2, m_norm_post2, m_w_in, m_hg_lb, m_hg_onorm, m_gla_w_gk, m_gla_b_gk, m_gla_onorm, m_w_br_hg, m_w_br_gla, m_w_out, m_w_ff_gate, m_w_ff_up, m_w_ff_down, v_c_ctx, v_w_mod, v_b_mod, v_norm_pre1, v_norm_post1, v_norm_pre2, v_norm_post2, v_w_in, v_hg_lb, v_hg_onorm, v_gla_w_gk, v_gla_b_gk, v_gla_onorm, v_w_br_hg, v_w_br_gla, v_w_out, v_w_ff_gate, v_w_ff_up, v_w_ff_down):
    xi, yi, ci = lax.axis_index("x"), lax.axis_index("y"), lax.axis_index("c")
    me = 4 * xi + 2 * yi + ci
    t = CTX + x.shape[1]

    c_all, lb_g, wgk_g, bgk_g = _all_gather([c, hg_lb, gla_w_gk[0], gla_b_gk[0]], "ag_small")
    tr_ = lambda a: jnp.swapaxes(a[0], 0, 1)
    big = [w_in[0], w_br_hg[0], w_br_gla[0], w_out[0], tr_(w_ff_gate), tr_(w_ff_up), w_ff_down[0]]
    big_bf = [w.astype(BF16) for w in big]
    g_in, = _all_gather(big_bf[:1], "ag_w_in")
    mix_handle, tok = _split_start(True, big_bf[1:4], "ag_mix_start", g_in)
    ffn_handle, tok = _split_start(True, big_bf[4:], "ag_ffn_start", tok)
    cols = lambda g: jnp.transpose(g, (1, 0, 2)).reshape(g.shape[1], N_DEV * g.shape[2])
    w_in_k = _tie(_layout_w_in(cols(g_in)), tok, "tie_w_in")

    def get_mix(after):
        g_brh, g_brg, g_out = _split_wait(mix_handle, "ag_mix_wait", after)
        return cols(g_brh), cols(g_brg), g_out.reshape(D, D)

    def get_ffn(after):
        g_gate, g_up, g_down = _split_wait(ffn_handle, "ag_ffn_wait", after)
        return jnp.concatenate([g_gate.reshape(D_FF, D), g_up.reshape(D_FF, D)], axis=0), g_down.reshape(D_FF, D)

    hg_lb_full = jnp.transpose(lb_g, (1, 2, 0, 3)).reshape(2, 2, HW)
    wgk_k = _layout_wgk(jnp.transpose(wgk_g, (1, 2, 0, 3)).reshape(2, 16, HW)).astype(BF16)
    bgk_k = jnp.transpose(bgk_g, (1, 0, 2)).reshape(1, D)
    onw = jnp.concatenate([jnp.tile(hg_onorm, (1, NH // 2)), jnp.tile(gla_onorm, (1, NH // 2))], axis=1)

    n_mod = w_mod.shape[2]
    a9 = jnp.concatenate([c_ctx[None], c_all[:, 0], jnp.zeros((16 - 1 - N_DEV, D), F32)], axis=0)
    b_loc = lax.dynamic_slice(b_mod, (0, me * n_mod), (1, n_mod))
    s_loc = _mod_fwd(a9, w_mod[0], b_loc)
    s_all, = _all_gather([s_loc], "ag_mod")
    mod_all = jnp.transpose(s_all, (1, 0, 2)).reshape(16, N_DEV * n_mod)
    pad8 = lambda m: jnp.concatenate([m.reshape(6, D), jnp.zeros((2, D), F32)], axis=0)
    modc = pad8(mod_all[0])
    modx = pad8(lax.dynamic_slice(mod_all, (1 + me, 0), (1, N_DEV * n_mod))[0])

    z = jnp.concatenate([ctx[0], x[0]], axis=0)
    norms = (norm_pre1, norm_post1, norm_pre2, norm_post2)
    shard = lambda d: jnp.transpose(d.reshape(d.shape[0], N_DEV, -1), (1, 0, 2)).astype(BF16)
    rowshard = lambda d: d.reshape(N_DEV, d.shape[0] // N_DEV, d.shape[1]).astype(BF16)
    sent, pending = [], []

    def send(names, grads, x_after):
        arrs, leaves = [], []
        for nm, g in zip(names, grads):
            if nm == "w_gu_t":
                arrs += [rowshard(g[:D_FF]), rowshard(g[D_FF:])]
                leaves += ["w_ff_gate", "w_ff_up"]
            elif nm == "w_in":
                full = shard(_unlayout_w_in(g))
                chunks = [full[:, a:b] for a, b in W_IN_GRAD_ROWS]
                arrs.append(chunks[0])
                leaves.append("w_in#0")
                pending.extend(chunks[1:])
            elif nm in ("w_out", "w_down"):
                arrs.append(rowshard(g))
                leaves.append({"w_down": "w_ff_down"}.get(nm, nm))
            else:
                arrs.append(shard(g))
                leaves.append(nm)
        handle, tok = _split_start(False, arrs, "grads_%s_start" % names[0], x_after)
        sent.append((names[0], leaves, handle))
        return _tie(x_after, tok, "tie_" + names[0])

    def send_pending(i, x_after):
        handle, tok = _split_start(False, [pending[i - 1]], "grads_w_in%d_start" % i, x_after)
        sent.append(("w_in%d" % i, ["w_in#%d" % i], handle))
        return _tie(x_after, tok, "tie_w_in%d" % i)

    r = _local_step(z, loss_target[0], modc, modx, norms, onw, hg_lb_full, wgk_k, bgk_k,
                    w_in_k, get_mix, get_ffn, send)
    loss = lax.psum((0.5 / D) * jnp.sum(r["loss_vec"]), ("x", "y", "c"))
    grad_x = r["grad_x"][None]

    sm_pre, sm_mid, sm_fin = r["sm_pre"], r["sm_mid"], r["sm_final"]
    dmodc = jnp.stack([sm_pre[0], sm_pre[2], sm_mid[4], sm_mid[0], sm_mid[2], sm_fin[0]]).reshape(-1)
    dmodx = jnp.stack([sm_pre[1], sm_pre[3], sm_mid[5], sm_mid[1], sm_mid[3], sm_fin[1]]).reshape(-1)
    on = r["sm_post"][0].reshape(NH, DH)
    pieces = [dmodc, dmodx, sm_pre[4], sm_mid[7], sm_mid[6], sm_fin[2], on[:NH // 2].sum(0), on[NH // 2:].sum(0),
              r["d_lb"][:2].reshape(-1), _unlayout_wgk(r["d_wgk"]).reshape(-1), r["d_bgk"][0]]
    sizes = [p.shape[0] for p in pieces]
    pack = jnp.concatenate(pieces).reshape(-1, DH)
    pack_all, = _all_gather([pack], "ag_small_grads")
    pack_all = send_pending(1, pack_all)
    tot = _sum_devices(pack_all).reshape(-1)
    offs = [sum(sizes[:i]) for i in range(len(sizes))]
    part = lambda i: tot[offs[i]:offs[i] + sizes[i]]
    dmodc_t, dmodx_t = part(0), part(1)
    g_b_mod = (dmodc_t + dmodx_t)[None]
    g_norms = [part(i)[None] for i in (2, 3, 4, 5)]
    g_hg_on, g_gla_on = part(6)[None], part(7)[None]
    lb0 = lax.dynamic_slice(part(8).reshape(2, HW), (0, me * (HW // N_DEV)), (2, HW // N_DEV))
    g_hg_lb = jnp.stack([lb0, -lb0])
    g_wgk = lax.dynamic_slice(part(9).reshape(2, 16, HW), (0, 0, me * (HW // N_DEV)), (2, 16, HW // N_DEV))[None]
    g_bgk = lax.dynamic_slice(part(10).reshape(2, HW), (0, me * (HW // N_DEV)), (2, HW // N_DEV))[None]

    dmx_all = pack_all.reshape(N_DEV, -1)[:, sizes[0]:sizes[0] + sizes[1]]
    d9 = jnp.concatenate([lax.dynamic_slice(dmodc_t[None], (0, me * n_mod), (1, n_mod)),
                          lax.dynamic_slice(dmx_all, (0, me * n_mod), (N_DEV, n_mod)),
                          jnp.zeros((16 - 1 - N_DEV, n_mod), F32)], axis=0)
    g_w_mod, dcc_part = _mod_bwd(a9, d9, w_mod[0])
    dcc_all, = _all_gather([dcc_part], "ag_c_ctx")
    dcc_all = send_pending(2, dcc_all)
    g_c_ctx = _sum_devices(dcc_all)[0]

    recv = {}
    for first, leaves, handle in sent:
        after = g_c_ctx if first.startswith("w_in") else r["grad_x"]
        recv.update(zip(leaves, _split_wait(handle, "grads_%s_wait" % first, after)))
    recv["w_in"] = jnp.concatenate([recv.pop("w_in#%d" % i) for i in range(len(W_IN_GRAD_ROWS))], axis=1)
    moms = [(m_w_in, v_w_in), (m_w_br_hg, v_w_br_hg), (m_w_br_gla, v_w_br_gla), (m_w_out, v_w_out),
            (m_w_ff_gate, v_w_ff_gate), (m_w_ff_up, v_w_ff_up), (m_w_ff_down, v_w_ff_down)]
    names = ["w_in", "w_br_hg", "w_br_gla", "w_out", "w_ff_gate", "w_ff_up", "w_ff_down"]
    res = {}
    for nm, w, (m, v) in zip(names, big, moms):
        if nm in ("w_ff_gate", "w_ff_up"):
            outs = _adamw(recv[nm], w, tr_(m), tr_(v), "adamw_" + nm)
            res[nm] = [jnp.swapaxes(o, 0, 1)[None] for o in outs]
        else:
            res[nm] = [o[None] for o in _adamw(recv[nm], w, m[0], v[0], "adamw_" + nm)]
    res["w_mod"] = [o[None] for o in _adamw(g_w_mod[None], w_mod[0], m_w_mod[0], v_w_mod[0], "adamw_w_mod")]

    small = [("c_ctx", c_ctx, m_c_ctx, v_c_ctx, g_c_ctx), ("b_mod", b_mod, m_b_mod, v_b_mod, g_b_mod),
             ("norm_pre1", norm_pre1, m_norm_pre1, v_norm_pre1, g_norms[0]),
             ("norm_post1", norm_post1, m_norm_post1, v_norm_post1, g_norms[1]),
             ("norm_pre2", norm_pre2, m_norm_pre2, v_norm_pre2, g_norms[2]),
             ("norm_post2", norm_post2, m_norm_post2, v_norm_post2, g_norms[3]),
             ("hg_lb", hg_lb, m_hg_lb, v_hg_lb, g_hg_lb), ("hg_onorm", hg_onorm, m_hg_onorm, v_hg_onorm, g_hg_on),
             ("gla_w_gk", gla_w_gk, m_gla_w_gk, v_gla_w_gk, g_wgk), ("gla_b_gk", gla_b_gk, m_gla_b_gk, v_gla_b_gk, g_bgk),
             ("gla_onorm", gla_onorm, m_gla_onorm, v_gla_onorm, g_gla_on)]
    flat = lambda k: jnp.concatenate([s[k].reshape(-1) for s in small]).reshape(-1, DH)
    outs = _adamw(flat(4)[None], flat(1), flat(2), flat(3), "adamw_small")
    off = 0
    for nm, w, _, _, _ in small:
        res[nm] = [o.reshape(-1)[off:off + w.size].reshape(w.shape) for o in outs]
        off += w.size

    order = ["c_ctx", "w_mod", "b_mod", "norm_pre1", "norm_post1", "norm_pre2", "norm_post2", "w_in", "hg_lb",
             "hg_onorm", "gla_w_gk", "gla_b_gk", "gla_onorm", "w_br_hg", "w_br_gla", "w_out", "w_ff_gate", "w_ff_up",
             "w_ff_down"]
    return (loss, grad_x, *[res[n][k] for k in range(4) for n in order])
```

```python
import functools

import jax
import jax.numpy as jnp
from jax import lax
from jax.experimental import pallas as pl
from jax.experimental.pallas import tpu as pltpu

F32 = jnp.float32
BF16 = jnp.bfloat16
HI = lax.Precision.HIGHEST

N_DEV = 8
D = 1024
CTX = 256
HW = 512
DH = 128
NH = 8
D_FF = 2816
EPS = 1e-6
GLA_NORM = 16.0
CHUNK = 64
TR = 256
NCT = CTX // TR
W_IN_COLS = 7168
MAIN0 = 0
LR0 = 4608
GW = 1152
GOFF = 32
GATE_HG0 = LR0
GATE_GLA0 = LR0 + D
LEVELS = (32, 16, 8)
EXP_CLAMP = 80.0
VMEM_LIMIT = 48 * 1024 * 1024

ADAM_LR, ADAM_B1, ADAM_B2, ADAM_EPS, ADAM_WD, ADAM_STEP = 0.001, 0.9, 0.999, 1e-08, 0.01, 10


def _cp(*sem):
    return pltpu.CompilerParams(dimension_semantics=sem, vmem_limit_bytes=VMEM_LIMIT)


def _sig(x):
    return jax.nn.sigmoid(x)


def _silu(x):
    return x * _sig(x)


def _dsilu(x):
    s = _sig(x)
    return s * (1.0 + x * (1.0 - s))


def _rstd(x):
    return lax.rsqrt(jnp.mean(x * x, axis=-1, keepdims=True) + EPS)


def _rms_bwd(a, y, r):
    return r * (a - y * (r * r) * jnp.mean(a * y, axis=-1, keepdims=True))


def _colsum(x):
    return jnp.sum(x, axis=0, keepdims=True)


def _dot(a, b, dims, precision=None):
    return lax.dot_general(a, b, (dims, ((), ())), preferred_element_type=F32, precision=precision)


NN = ((1,), (0,))
NT = ((1,), (1,))
TN = ((0,), (0,))

SCAN_HEADS_FWD = 4
SCAN_HEADS_BWD = 4


def _split_dot(m, x):
    mb = m.astype(BF16)
    x1 = x.astype(BF16)
    r1 = x - x1.astype(F32)
    x2 = r1.astype(BF16)
    x3 = (r1 - x2.astype(F32)).astype(BF16)
    return _dot(mb, x1, NN) + _dot(mb, x2, NN) + _dot(mb, x3, NN)


def _matmul(a, b, dims, out_dtype, name, tm, tn, tk, a_off=0, m_out=None):
    if dims == NN:
        m, k = a.shape[0], b.shape[0]
        n = b.shape[1]
        a_spec = pl.BlockSpec((tm, tk), lambda i, j, kk: (i, kk + a_off))
        b_spec = pl.BlockSpec((tk, tn), lambda i, j, kk: (kk, j))
    elif dims == NT:
        m, k = a.shape[0], b.shape[1]
        n = b.shape[0]
        a_spec = pl.BlockSpec((tm, tk), lambda i, j, kk: (i, kk + a_off))
        b_spec = pl.BlockSpec((tn, tk), lambda i, j, kk: (j, kk))
    else:
        m, k = (a.shape[1] if m_out is None else m_out), a.shape[0]
        n = b.shape[1]
        a_spec = pl.BlockSpec((tk, tm), lambda i, j, kk: (kk, i + a_off))
        b_spec = pl.BlockSpec((tk, tn), lambda i, j, kk: (kk, j))
    assert m % tm == 0 and n % tn == 0 and k % tk == 0, (name, m, n, k, tm, tn, tk)
    nk = k // tk

    def body(a_ref, b_ref, o_ref, *acc):
        part = _dot(a_ref[...], b_ref[...], dims)
        if nk == 1:
            o_ref[...] = part.astype(o_ref.dtype)
            return
        acc_ref, = acc
        kk = pl.program_id(2)

        @pl.when(kk == 0)
        def _():
            acc_ref[...] = part

        @pl.when(kk > 0)
        def _():
            acc_ref[...] += part

        @pl.when(kk == nk - 1)
        def _():
            o_ref[...] = acc_ref[...].astype(o_ref.dtype)

    return pl.pallas_call(
        body,
        name=name,
        grid=(m // tm, n // tn, nk),
        in_specs=[a_spec, b_spec],
        out_specs=pl.BlockSpec((tm, tn), lambda i, j, kk: (i, j)),
        out_shape=jax.ShapeDtypeStruct((m, n), out_dtype),
        scratch_shapes=[] if nk == 1 else [pltpu.VMEM((tm, tn), F32)],
        compiler_params=_cp("parallel", "parallel", "arbitrary"),
    )(a, b)


def _row(c):
    return pl.BlockSpec((TR, c), lambda i: (i, 0))


def _rowcol(width, cb):
    return pl.BlockSpec((TR, width), lambda i: (i, cb))


def _full(shape):
    return pl.BlockSpec(shape, lambda i: (0,) * len(shape))


def _mod_row(mc_ref, mx_ref, k, is_ctx):
    return jnp.where(is_ctx, mc_ref[k:k + 1, :], mx_ref[k:k + 1, :])


def _acc_row(ref, k, val):
    ref[k:k + 1, :] += val


def _acc_mod(ref, k, is_ctx, val):
    zero = jnp.zeros_like(val)
    ref[k:k + 1, :] += jnp.where(is_ctx, val, zero)
    ref[k + 1:k + 2, :] += jnp.where(is_ctx, zero, val)


def _prenorm(z, nw, modc, modx, i_shift, i_scale, name):
    t = z.shape[0]

    def body(z_ref, nw_ref, mc_ref, mx_ref, h_ref):
        is_ctx = pl.program_id(0) < NCT
        x = z_ref[...]
        n = x * _rstd(x) * nw_ref[...]
        h = n * (1.0 + _mod_row(mc_ref, mx_ref, i_scale, is_ctx)) + _mod_row(mc_ref, mx_ref, i_shift, is_ctx)
        h_ref[...] = h.astype(BF16)

    return pl.pallas_call(
        body, name=name, grid=(t // TR,),
        in_specs=[_row(D), _full((1, D)), _full((8, D)), _full((8, D))],
        out_specs=_row(D),
        out_shape=jax.ShapeDtypeStruct((t, D), BF16),
        compiler_params=_cp("parallel"),
    )(z, nw, modc, modx)


def _hg_lb(lb_ref, d):
    a0 = lb_ref[0, d:d + 1, :]
    a1 = lb_ref[1, d:d + 1, :]
    mx = jnp.maximum(a0, a1)
    e0 = jnp.exp(a0 - mx)
    e1 = jnp.exp(a1 - mx)
    return e0 / (e0 + e1)


def _log_sigmoid(x):
    return jnp.minimum(x, 0.0) - jnp.log(1.0 + jnp.exp(-jnp.abs(x)))


def _gates_fwd(p, hg_lb, wgk, bgk):
    t = p.shape[0]
    seg = lambda j: _rowcol(HW, MAIN0 // HW + j)

    def body(hq_ref, hi_ref, hf_ref, hb_ref, gq_ref, gk_ref, gv_ref, lr_ref, lb_ref, wgk_ref, bgk_ref,
             q_ref, v_ref, kf_ref, kb_ref, gf_ref, gb_ref):
        q_ref[:, :HW] = _silu(hq_ref[...].astype(F32))
        q_ref[:, HW:] = gq_ref[...].astype(F32) * (DH ** -0.5)
        v_ref[:, :HW] = hi_ref[...].astype(F32)
        v_ref[:, HW:] = gv_ref[...].astype(F32)
        xg = _dot(lr_ref[...].astype(BF16), wgk_ref[...], NN) + bgk_ref[...]
        for d, (raw_ref, k_ref, g_ref) in enumerate(((hf_ref, kf_ref, gf_ref), (hb_ref, kb_ref, gb_ref))):
            lbd = _hg_lb(lb_ref, d)
            f = lbd + (1.0 - lbd) * _sig(raw_ref[...].astype(F32))
            k_ref[:, :HW] = 1.0 - f
            k_ref[:, HW:] = gk_ref[...].astype(F32)
            g_ref[:, :HW] = jnp.log(f)
            g_ref[:, HW:] = _log_sigmoid(xg[:, d * HW:(d + 1) * HW]) * (1.0 / GLA_NORM)

    out = jax.ShapeDtypeStruct((t, D), F32)
    return pl.pallas_call(
        body, name="gates_fwd", grid=(t // TR,),
        in_specs=[seg(0), seg(1), seg(2), seg(3), seg(5), seg(6), seg(7), _rowcol(DH, LR0 // DH),
                  _full((2, 2, HW)), _full((DH, D)), _full((1, D))],
        out_specs=[_row(D)] * 6,
        out_shape=[out] * 6,
        compiler_params=_cp("parallel"),
    )(p, p, p, p, p, p, p, p, hg_lb, wgk, bgk)


def _post_fwd(o_fw, o_bw, p, onw):
    t = o_fw.shape[0]

    def body(of_ref, ob_ref, g1_ref, g2_ref, w_ref, y_ref):
        for h in range(NH):
            sl = slice(h * DH, (h + 1) * DH)
            o = of_ref[:, sl] + ob_ref[:, sl]
            g_ref = g1_ref if h < NH // 2 else g2_ref
            gs = slice((h % (NH // 2)) * DH, (h % (NH // 2) + 1) * DH)
            n = o * _rstd(o) * w_ref[:, sl]
            y_ref[:, sl] = (n * _silu(g_ref[:, gs].astype(F32))).astype(BF16)

    return pl.pallas_call(
        body, name="post_fwd", grid=(t // TR,),
        in_specs=[_row(D), _row(D), _rowcol(HW, MAIN0 // HW + 4), _rowcol(HW, MAIN0 // HW + 8), _full((1, D))],
        out_specs=_row(D),
        out_shape=jax.ShapeDtypeStruct((t, D), BF16),
        compiler_params=_cp("parallel"),
    )(o_fw, o_bw, p, p, onw)


def _gate_window_specs(col0):
    return [_rowcol(HW, col0 // HW), _rowcol(HW, col0 // HW + 1), _rowcol(DH, (col0 + 2 * HW) // DH)]


def _gate_window(refs):
    return jnp.concatenate([r[...].astype(F32) for r in refs], axis=1)


def _merge_fwd(p, u1, u2):
    t = p.shape[0]

    def body(a0, a1, a2, b0, b1, b2, u1_ref, u2_ref, m_ref):
        f = lambda r: r[...].astype(F32)
        m_ref[...] = (_sig(_gate_window((a0, a1, a2))) * f(u1_ref)
                      + _sig(_gate_window((b0, b1, b2))) * f(u2_ref)).astype(BF16)

    return pl.pallas_call(
        body, name="merge_fwd", grid=(t // TR,),
        in_specs=_gate_window_specs(GATE_HG0) + _gate_window_specs(GATE_GLA0) + [_row(GW), _row(GW)],
        out_specs=_row(GW),
        out_shape=jax.ShapeDtypeStruct((t, GW), BF16),
        compiler_params=_cp("parallel"),
    )(p, p, p, p, p, p, u1, u2)


def _mid_fwd(z, y1, nw_post, nw_pre, modc, modx):
    t = z.shape[0]

    def body(z_ref, y_ref, wpo_ref, wpr_ref, mc_ref, mx_ref, z1_ref, h_ref):
        is_ctx = pl.program_id(0) < NCT
        y = y_ref[...]
        z1 = z_ref[...] + _mod_row(mc_ref, mx_ref, 2, is_ctx) * (y * _rstd(y) * wpo_ref[...])
        z1_ref[...] = z1
        n = z1 * _rstd(z1) * wpr_ref[...]
        h = n * (1.0 + _mod_row(mc_ref, mx_ref, 4, is_ctx)) + _mod_row(mc_ref, mx_ref, 3, is_ctx)
        h_ref[...] = h.astype(BF16)

    return pl.pallas_call(
        body, name="mid_fwd", grid=(t // TR,),
        in_specs=[_row(D), _row(D), _full((1, D)), _full((1, D)), _full((8, D)), _full((8, D))],
        out_specs=[_row(D), _row(D)],
        out_shape=[jax.ShapeDtypeStruct((t, D), F32), jax.ShapeDtypeStruct((t, D), BF16)],
        compiler_params=_cp("parallel"),
    )(z, y1, nw_post, nw_pre, modc, modx)


def _swiglu_fwd(uv):
    t = uv.shape[0]

    def body(u_ref, v_ref, a_ref):
        a_ref[...] = (_silu(u_ref[...].astype(F32)) * v_ref[...].astype(F32)).astype(BF16)

    return pl.pallas_call(
        body, name="swiglu_fwd", grid=(t // TR,),
        in_specs=[_rowcol(D_FF, 0), _rowcol(D_FF, 1)],
        out_specs=_row(D_FF),
        out_shape=jax.ShapeDtypeStruct((t, D_FF), BF16),
        compiler_params=_cp("parallel"),
    )(uv, uv)


def _swiglu_bwd(uv, da):
    t = uv.shape[0]

    def body(u_ref, v_ref, da_ref, d_ref):
        u = u_ref[...].astype(F32)
        d = da_ref[...].astype(F32)
        d_ref[:, :D_FF] = (d * v_ref[...].astype(F32) * _dsilu(u)).astype(BF16)
        d_ref[:, D_FF:] = (d * _silu(u)).astype(BF16)

    return pl.pallas_call(
        body, name="swiglu_bwd", grid=(t // TR,),
        in_specs=[_rowcol(D_FF, 0), _rowcol(D_FF, 1), _row(D_FF)],
        out_specs=_row(2 * D_FF),
        out_shape=jax.ShapeDtypeStruct((t, 2 * D_FF), BF16),
        compiler_params=_cp("parallel"),
    )(uv, uv, da)


def _final(z1, y2, target, nw, modc, modx):
    t = z1.shape[0]

    def body(z1_ref, y_ref, tg_ref, w_ref, mc_ref, mx_ref, dz_ref, dy_ref, loss_ref, sm_ref):
        i = pl.program_id(0)
        is_ctx = i < NCT

        @pl.when(i == 0)
        def _():
            loss_ref[...] = jnp.zeros_like(loss_ref)
            sm_ref[...] = jnp.zeros_like(sm_ref)

        g = _mod_row(mc_ref, mx_ref, 5, is_ctx)
        y = y_ref[...]
        r = _rstd(y)
        w = w_ref[...]
        yr = y * r
        n = yr * w
        e = z1_ref[...] + g * n - tg_ref[...]
        lat = jnp.where(is_ctx, 0.0, 1.0)
        loss_ref[...] += lat * _colsum(e * e)
        dz = e * (lat / D)
        dz_ref[...] = dz
        _acc_mod(sm_ref, 0, is_ctx, _colsum(dz * n))
        dn = dz * g
        _acc_row(sm_ref, 2, _colsum(dn * yr))
        dy_ref[...] = _rms_bwd(dn * w, y, r).astype(BF16)

    return pl.pallas_call(
        body, name="final", grid=(t // TR,),
        in_specs=[_row(D), _row(D), pl.BlockSpec((TR, D), lambda i: (jnp.maximum(i - NCT, 0), 0)),
                  _full((1, D)), _full((8, D)), _full((8, D))],
        out_specs=[_row(D), _row(D), _full((1, D)), _full((8, D))],
        out_shape=[jax.ShapeDtypeStruct((t, D), F32), jax.ShapeDtypeStruct((t, D), BF16),
                   jax.ShapeDtypeStruct((1, D), F32), jax.ShapeDtypeStruct((8, D), F32)],
        compiler_params=_cp("arbitrary"),
    )(z1, y2, target, nw, modc, modx)


def _mid_bwd(dh2, dz, z, z1, y1, nw_post, nw_pre, modc, modx):
    t = z.shape[0]

    def body(dh_ref, dz_ref, z_ref, z1_ref, y_ref, wpo_ref, wpr_ref, mc_ref, mx_ref, dzo_ref, dy_ref, sm_ref):
        i = pl.program_id(0)
        is_ctx = i < NCT

        @pl.when(i == 0)
        def _():
            sm_ref[...] = jnp.zeros_like(sm_ref)

        dh = dh_ref[...]
        z1 = z1_ref[...]
        r = _rstd(z1)
        zr = z1 * r
        wpr = wpr_ref[...]
        n = zr * wpr
        _acc_mod(sm_ref, 0, is_ctx, _colsum(dh))
        _acc_mod(sm_ref, 2, is_ctx, _colsum(dh * n))
        dn = dh * (1.0 + _mod_row(mc_ref, mx_ref, 4, is_ctx))
        _acc_row(sm_ref, 6, _colsum(dn * zr))
        dz1 = dz_ref[...] + _rms_bwd(dn * wpr, z1, r)
        dzo_ref[...] = dz1
        y = y_ref[...]
        r1 = _rstd(y)
        yr = y * r1
        wpo = wpo_ref[...]
        g = _mod_row(mc_ref, mx_ref, 2, is_ctx)
        _acc_mod(sm_ref, 4, is_ctx, _colsum(dz1 * (yr * wpo)))
        dn1 = dz1 * g
        _acc_row(sm_ref, 7, _colsum(dn1 * yr))
        dy_ref[...] = _rms_bwd(dn1 * wpo, y, r1).astype(BF16)

    return pl.pallas_call(
        body, name="mid_bwd", grid=(t // TR,),
        in_specs=[_row(D)] * 5 + [_full((1, D)), _full((1, D)), _full((8, D)), _full((8, D))],
        out_specs=[_row(D), _row(D), _full((8, D))],
        out_shape=[jax.ShapeDtypeStruct((t, D), F32), jax.ShapeDtypeStruct((t, D), BF16),
                   jax.ShapeDtypeStruct((8, D), F32)],
        compiler_params=_cp("arbitrary"),
    )(dh2, dz, z, z1, y1, nw_post, nw_pre, modc, modx)


def _pre_bwd(dh1, dz, z, nw, modc, modx):
    t = z.shape[0]

    def body(dh_ref, dz_ref, z_ref, w_ref, mc_ref, mx_ref, dzo_ref, sm_ref):
        i = pl.program_id(0)
        is_ctx = i < NCT

        @pl.when(i == 0)
        def _():
            sm_ref[...] = jnp.zeros_like(sm_ref)

        dh = dh_ref[...]
        x = z_ref[...]
        r = _rstd(x)
        xr = x * r
        w = w_ref[...]
        _acc_mod(sm_ref, 0, is_ctx, _colsum(dh))
        _acc_mod(sm_ref, 2, is_ctx, _colsum(dh * (xr * w)))
        dn = dh * (1.0 + _mod_row(mc_ref, mx_ref, 1, is_ctx))
        _acc_row(sm_ref, 4, _colsum(dn * xr))
        dzo_ref[...] = dz_ref[...] + _rms_bwd(dn * w, x, r)

    return pl.pallas_call(
        body, name="pre_bwd", grid=(t // TR,),
        in_specs=[_row(D)] * 3 + [_full((1, D)), _full((8, D)), _full((8, D))],
        out_specs=[pl.BlockSpec((TR, D), lambda i: (jnp.maximum(i - NCT, 0), 0)), _full((8, D))],
        out_shape=[jax.ShapeDtypeStruct((t - CTX, D), F32), jax.ShapeDtypeStruct((8, D), F32)],
        compiler_params=_cp("arbitrary"),
    )(dh1, dz, z, nw, modc, modx)


def _merge_bwd(dm, p, u1, u2):
    t = dm.shape[0]

    def body(dm_ref, a0, a1, a2, b0, b1, b2, u1_ref, u2_ref, du1_ref, du2_ref, dg_ref):
        dm_ = dm_ref[...]
        s1 = _sig(_gate_window((a0, a1, a2)))
        s2 = _sig(_gate_window((b0, b1, b2)))
        du1_ref[...] = (dm_ * s1).astype(BF16)
        du2_ref[...] = (dm_ * s2).astype(BF16)
        dg_ref[:, :GW] = (dm_ * u1_ref[...].astype(F32) * s1 * (1.0 - s1)).astype(BF16)
        dg_ref[:, GW:] = (dm_ * u2_ref[...].astype(F32) * s2 * (1.0 - s2)).astype(BF16)

    return pl.pallas_call(
        body, name="merge_bwd", grid=(t // TR,),
        in_specs=[_row(GW)] + _gate_window_specs(GATE_HG0) + _gate_window_specs(GATE_GLA0) + [_row(GW), _row(GW)],
        out_specs=[_row(GW), _row(GW), _row(2 * GW)],
        out_shape=[jax.ShapeDtypeStruct((t, GW), BF16), jax.ShapeDtypeStruct((t, GW), BF16),
                   jax.ShapeDtypeStruct((t, 2 * GW), BF16)],
        compiler_params=_cp("parallel"),
    )(dm, p, p, p, p, p, p, u1, u2)


def _post_bwd(dy_hg, dy_gla, o_fw, o_bw, p, onw):
    t = o_fw.shape[0]

    def body(d1_ref, d2_ref, of_ref, ob_ref, g1_ref, g2_ref, w_ref, do_ref, dg_ref, sm_ref):
        @pl.when(pl.program_id(0) == 0)
        def _():
            sm_ref[...] = jnp.zeros_like(sm_ref)

        for h in range(NH):
            sl = slice(h * DH, (h + 1) * DH)
            gs = slice((h % (NH // 2)) * DH, (h % (NH // 2) + 1) * DH)
            g_ref, d_ref = (g1_ref, d1_ref) if h < NH // 2 else (g2_ref, d2_ref)
            o = of_ref[:, sl] + ob_ref[:, sl]
            r = _rstd(o)
            orr = o * r
            w = w_ref[:, sl]
            gt = g_ref[:, gs].astype(F32)
            dy = d_ref[:, gs]
            dg_ref[:, sl] = (dy * (orr * w) * _dsilu(gt)).astype(BF16)
            dn = dy * _silu(gt)
            sm_ref[0:1, sl] += _colsum(dn * orr)
            do_ref[:, sl] = _rms_bwd(dn * w, o, r)

    return pl.pallas_call(
        body, name="post_bwd", grid=(t // TR,),
        in_specs=[_row(HW), _row(HW), _row(D), _row(D), _rowcol(HW, MAIN0 // HW + 4), _rowcol(HW, MAIN0 // HW + 8),
                  _full((1, D))],
        out_specs=[_row(D), _row(D), _full((8, D))],
        out_shape=[jax.ShapeDtypeStruct((t, D), F32), jax.ShapeDtypeStruct((t, D), BF16),
                   jax.ShapeDtypeStruct((8, D), F32)],
        compiler_params=_cp("arbitrary"),
    )(dy_hg, dy_gla, o_fw, o_bw, p, p, onw)


def _gates_bwd(p, hg_lb, wgk, bgk, dgm, dgo, dq_f, dq_b, dv_f, dv_b, dk_f, dk_b, dg_f, dg_b):
    t = p.shape[0]
    seg = lambda j: _rowcol(HW, MAIN0 // HW + j)

    def body(hq_ref, hf_ref, hb_ref, lr_ref, lb_ref, wgk_ref, bgk_ref, dgm_ref, dgo_ref,
             dqf_ref, dqb_ref, dvf_ref, dvb_ref, dkf_ref, dkb_ref, dgf_ref, dgb_ref,
             dp_ref, dlb_ref, dw_ref, db_ref):
        @pl.when(pl.program_id(0) == 0)
        def _():
            dlb_ref[...] = jnp.zeros_like(dlb_ref)
            dw_ref[...] = jnp.zeros_like(dw_ref)
            db_ref[...] = jnp.zeros_like(db_ref)

        c0 = MAIN0

        def put(j, val):
            dp_ref[:, c0 + j * HW:c0 + (j + 1) * HW] = val.astype(BF16)

        dq = dqf_ref[...] + dqb_ref[...]
        dv = dvf_ref[...] + dvb_ref[...]
        put(0, dq[:, :HW] * _dsilu(hq_ref[...].astype(F32)))
        put(1, dv[:, :HW])
        put(5, dq[:, HW:] * (DH ** -0.5))
        put(7, dv[:, HW:])
        put(6, dkf_ref[:, HW:] + dkb_ref[:, HW:])
        dp_ref[:, c0 + 4 * HW:c0 + 5 * HW] = dgo_ref[:, :HW]
        dp_ref[:, c0 + 8 * HW:c0 + 9 * HW] = dgo_ref[:, HW:]
        lr = lr_ref[...].astype(BF16)
        xg = _dot(lr, wgk_ref[...], NN) + bgk_ref[...]
        dxg = []
        for d, (raw_ref, dk_ref, dg_ref) in enumerate(((hf_ref, dkf_ref, dgf_ref), (hb_ref, dkb_ref, dgb_ref))):
            lbd = _hg_lb(lb_ref, d)
            s = _sig(raw_ref[...].astype(F32))
            f = lbd + (1.0 - lbd) * s
            df = dg_ref[:, :HW] / f - dk_ref[:, :HW]
            put(2 + d, df * (1.0 - lbd) * s * (1.0 - s))
            dlb_ref[d:d + 1, :] += _colsum(df * (1.0 - s)) * (lbd * (1.0 - lbd))
            dxg.append(dg_ref[:, HW:] * (1.0 / GLA_NORM) * _sig(-xg[:, d * HW:(d + 1) * HW]))
        dxg = jnp.concatenate(dxg, axis=1)
        db_ref[0:1, :] += _colsum(dxg)
        dxg_b = dxg.astype(BF16)
        dw_ref[...] += _dot(lr, dxg_b, TN)
        dlr = _dot(dxg_b, wgk_ref[...], NT)
        dp_ref[:, LR0:LR0 + DH] = (dlr + dgm_ref[:, :DH].astype(F32)).astype(BF16)
        dp_ref[:, LR0 + DH:GATE_GLA0] = dgm_ref[:, DH:D]
        dp_ref[:, GATE_GLA0:GATE_GLA0 + DH] = dgm_ref[:, D:GW] + dgm_ref[:, GW:GW + DH]
        dp_ref[:, GATE_GLA0 + DH:GATE_GLA0 + GW] = dgm_ref[:, GW + DH:]
        dp_ref[:, GATE_GLA0 + GW:] = jnp.zeros((TR, W_IN_COLS - GATE_GLA0 - GW), BF16)

    return pl.pallas_call(
        body, name="gates_bwd", grid=(t // TR,),
        in_specs=[seg(0), seg(2), seg(3), _rowcol(DH, LR0 // DH), _full((2, 2, HW)), _full((DH, D)), _full((1, D)),
                  _row(2 * GW), _row(D)] + [_row(D)] * 8,
        out_specs=[_row(W_IN_COLS), _full((8, HW)), _full((DH, D)), _full((8, D))],
        out_shape=[jax.ShapeDtypeStruct((t, W_IN_COLS), BF16), jax.ShapeDtypeStruct((8, HW), F32),
                   jax.ShapeDtypeStruct((DH, D), F32), jax.ShapeDtypeStruct((8, D), F32)],
        compiler_params=_cp("arbitrary"),
    )(p, p, p, p, hg_lb, wgk, bgk, dgm, dgo, dq_f, dq_b, dv_f, dv_b, dk_f, dk_b, dg_f, dg_b)


def _scan_consts(rev):
    r = lax.broadcasted_iota(jnp.int32, (CHUNK, CHUNK), 0)
    u = lax.broadcasted_iota(jnp.int32, (CHUNK, CHUNK), 1)
    rp = lax.broadcasted_iota(jnp.int32, (CHUNK, 1), 0)
    if rev:
        r, u, rp = CHUNK - 1 - r, CHUNK - 1 - u, CHUNK - 1 - rp
    tri = jnp.where(u <= r, 1.0, 0.0).astype(F32)
    tri_t = jnp.where(r <= u, 1.0, 0.0).astype(F32)
    lv = []
    for b in LEVELS:
        sh = b.bit_length() - 1
        pair = ((r >> sh) == (u >> sh) + 1) & (((u >> sh) & 1) == 0)
        pair_t = ((u >> sh) == (r >> sh) + 1) & (((r >> sh) & 1) == 0)
        tside = ((rp >> sh) & 1) == 1
        lv.append((pair, pair_t, tside))
    bd = LEVELS[-1].bit_length() - 1
    diag = ((r >> bd) == (u >> bd)) & (u <= r)
    diag_t = ((r >> bd) == (u >> bd)) & (r <= u)
    return tri, tri_t, lv, diag, diag_t


def _row_of(pos, rev):
    return CHUNK - 1 - pos if rev else pos


def _chunk_terms(cum, b_scr, consts, rev):
    _, _, lv, _, _ = consts
    terms = []
    for b, (_, _, tside) in zip(LEVELS, lv):
        pieces = []
        for j in range(CHUNK // (2 * b)):
            row = _row_of(2 * b * j + b - 1, rev)
            pieces.append(jnp.broadcast_to(b_scr[row:row + 1, :], (2 * b, DH)))
        if rev:
            pieces = pieces[::-1]
        bnd = pieces[0] if len(pieces) == 1 else jnp.concatenate(pieces, axis=0)
        w = jnp.exp(jnp.minimum(jnp.where(tside, cum - bnd, bnd - cum), 0.0))
        wq = jnp.where(tside, w, 0.0)
        wk = jnp.where(tside, 0.0, w)
        terms.append((wq, wk))
    b = LEVELS[-1]
    pieces = []
    for j in range(CHUNK // b):
        if j == 0:
            pieces.append(jnp.zeros((b, DH), F32))
        else:
            row = _row_of(b * j - 1, rev)
            pieces.append(jnp.broadcast_to(b_scr[row:row + 1, :], (b, DH)))
    if rev:
        pieces = pieces[::-1]
    start = jnp.concatenate(pieces, axis=0)
    wq = jnp.exp(jnp.minimum(cum - start, 0.0))
    wk = jnp.exp(jnp.minimum(start - cum, EXP_CLAMP))
    terms.append((wq, wk))
    return terms


def _run_staged(units):
    live = list(units)
    while live:
        nxt = []
        for u in live:
            try:
                next(u)
                nxt.append(u)
            except StopIteration:
                pass
        live = nxt


SCAN_TB = 256
SCAN_CB = SCAN_TB // CHUNK


def _block_order(i, ntb, rev):
    nctx = CTX // SCAN_TB
    if not rev:
        return i
    return jnp.where(i < nctx, nctx - 1 - i, ntb - 1 - (i - nctx))


def _chunk_in_block(j, rev):
    return SCAN_CB - 1 - j if rev else j


def _scan_fwd(q, k, v, g, rev):
    t = q.shape[0]
    nc = t // CHUNK
    hpb = SCAN_HEADS_FWD

    def body(q_ref, k_ref, v_ref, g_ref, o_ref, st_ref, s_scr, b_scr):
        consts = _scan_consts(rev)
        _, _, lv, diag, _ = consts
        masks = [pair for pair, _, _ in lv] + [diag]

        @pl.when(pl.program_id(1) == 0)
        def _():
            s_scr[...] = jnp.zeros_like(s_scr)

        tri = consts[0]
        state = {hh: s_scr[hh] for hh in range(hpb)}

        def unit(hh, j):
            sl = slice(hh * DH, (hh + 1) * DH)
            c = _chunk_in_block(j, rev)
            rows = slice(c * CHUNK, (c + 1) * CHUNK)
            b_ref = b_scr.at[hh * SCAN_CB + j]
            qc, kc, vc, gc = q_ref[rows, sl], k_ref[rows, sl], v_ref[rows, sl], g_ref[rows, sl]
            cum = _split_dot(tri, gc)
            b_ref[...] = cum
            yield
            terms = _chunk_terms(cum, b_ref, consts, rev)
            ops = [((qc * wq).astype(BF16), (kc * wk).astype(BF16)) for wq, wk in terms]
            tot = _colsum(gc)
            qe = (qc * jnp.exp(cum)).astype(BF16)
            ke = (kc * jnp.exp(tot - cum)).astype(BF16)
            vb = vc.astype(BF16)
            yield
            scs = [_dot(qt, kt, NT) for qt, kt in ops]
            kv = _dot(vb, ke, TN)
            yield
            a = jnp.zeros((CHUNK, CHUNK), F32)
            for sc, m in zip(scs, masks):
                a = a + jnp.where(m, sc, 0.0)
            o_intra = _dot(a.astype(BF16), vb, NN)
            yield
            st = state[hh]
            st_ref[hh, c] = st
            o_ref[rows, sl] = o_intra + _dot(qe, st.astype(BF16), NT)
            state[hh] = st * jnp.exp(tot) + kv
            yield

        _run_staged([unit(hh, j) for hh in range(hpb) for j in range(SCAN_CB)])
        for hh in range(hpb):
            s_scr[hh] = state[hh]

    ntb = t // SCAN_TB
    col = pl.BlockSpec((SCAN_TB, hpb * DH), lambda h, i: (_block_order(i, ntb, rev), h))
    return pl.pallas_call(
        body, name="scan_fwd_" + ("bw" if rev else "fw"), grid=(NH // hpb, ntb),
        in_specs=[col] * 4,
        out_specs=[col, pl.BlockSpec((hpb, SCAN_CB, DH, DH), lambda h, i: (h, _block_order(i, ntb, rev), 0, 0))],
        out_shape=[jax.ShapeDtypeStruct((t, D), F32), jax.ShapeDtypeStruct((NH, nc, DH, DH), F32)],
        scratch_shapes=[pltpu.VMEM((hpb, DH, DH), F32), pltpu.VMEM((hpb * SCAN_CB, CHUNK, DH), F32)],
        compiler_params=_cp("parallel", "arbitrary"),
    )(q, k, v, g)


def _scan_bwd(q, k, v, g, do, states, rev):
    t = q.shape[0]
    nc = t // CHUNK
    hpb = SCAN_HEADS_BWD

    def body(q_ref, k_ref, v_ref, g_ref, do_ref, st_ref, dq_ref, dk_ref, dv_ref, dg_ref, ds_scr, b_scr):
        consts = _scan_consts(rev)
        _, tri_t, lv, diag, diag_t = consts
        masks = [(pair, pair_t) for pair, pair_t, _ in lv] + [(diag, diag_t)]
        @pl.when(pl.program_id(1) == 0)
        def _():
            ds_scr[...] = jnp.zeros_like(ds_scr)

        tri = consts[0]
        dstate = {hh: ds_scr[hh] for hh in range(hpb)}

        def unit(hh, jj):
            sl = slice(hh * DH, (hh + 1) * DH)
            c = _chunk_in_block(SCAN_CB - 1 - jj, rev)
            rows = slice(c * CHUNK, (c + 1) * CHUNK)
            b_ref = b_scr.at[hh * SCAN_CB + jj]
            qc, kc, vc, gc = q_ref[rows, sl], k_ref[rows, sl], v_ref[rows, sl], g_ref[rows, sl]
            dob = do_ref[rows, sl].astype(BF16)
            vb = vc.astype(BF16)
            cum = _split_dot(tri, gc)
            b_ref[...] = cum
            da = _dot(dob, vb, NT)
            da_t = _dot(vb, dob, NT)
            yield
            terms = _chunk_terms(cum, b_ref, consts, rev)
            ops = [((qc * wq).astype(BF16), (kc * wk).astype(BF16)) for wq, wk in terms]
            tot = _colsum(gc)
            e_tot = jnp.exp(tot)
            e_b = jnp.exp(cum)
            e_t = jnp.exp(tot - cum)
            qeb = (qc * e_b).astype(BF16)
            keb = (kc * e_t).astype(BF16)
            dal = [(jnp.where(m, da, 0.0).astype(BF16), jnp.where(m_t, da_t, 0.0).astype(BF16)) for m, m_t in masks]
            yield
            ats = [_dot(ktb, qtb, NT) for qtb, ktb in ops]
            dqts = [_dot(d, ktb, NN) for (d, _), (_, ktb) in zip(dal, ops)]
            dkts = [_dot(d_t, qtb, NN) for (_, d_t), (qtb, _) in zip(dal, ops)]
            qd = _dot(dob, qeb, TN)
            yield
            a_t = jnp.zeros((CHUNK, CHUNK), F32)
            dq = jnp.zeros((CHUNK, DH), F32)
            dk = jnp.zeros((CHUNK, DH), F32)
            db = jnp.zeros((CHUNK, DH), F32)
            for at, dqt, dkt, (wq, wk), (qtb, ktb), (_, m_t) in zip(ats, dqts, dkts, terms, ops, masks):
                a_t = a_t + jnp.where(m_t, at, 0.0)
                dq = dq + dqt * wq
                dk = dk + dkt * wk
                db = db + dqt * qtb.astype(F32) - dkt * ktb.astype(F32)
            dv_intra = _dot(a_t.astype(BF16), dob, NN)
            st = st_ref[hh, c]
            stb = st.astype(BF16)
            dqe = _dot(dob, stb, NN)
            yield
            dst = dstate[hh]
            dstb = dst.astype(BF16)
            dstate[hh] = dst * e_tot + qd
            dv_ref[rows, sl] = dv_intra + _dot(keb, dstb, NT)
            dke = _dot(vb, dstb, NN)
            yield
            qe = qeb.astype(F32)
            ke = keb.astype(F32)
            dq_ref[rows, sl] = dq + dqe * e_b
            dk_ref[rows, sl] = dk + dke * e_t
            db = db + dqe * qe - dke * ke
            dtot = _colsum(dstb.astype(F32) * stb.astype(F32)) * e_tot + _colsum(dke * ke)
            dg_ref[rows, sl] = _split_dot(tri_t, db) + dtot
            yield

        _run_staged([unit(hh, jj) for hh in range(hpb) for jj in range(SCAN_CB)])
        for hh in range(hpb):
            ds_scr[hh] = dstate[hh]

    ntb = t // SCAN_TB
    blk = lambda i: _block_order(ntb - 1 - i, ntb, rev)
    col = pl.BlockSpec((SCAN_TB, hpb * DH), lambda h, i: (blk(i), h))
    out = jax.ShapeDtypeStruct((t, D), F32)
    return pl.pallas_call(
        body, name="scan_bwd_" + ("bw" if rev else "fw"), grid=(NH // hpb, ntb),
        in_specs=[col] * 5 + [pl.BlockSpec((hpb, SCAN_CB, DH, DH), lambda h, i: (h, blk(i), 0, 0))],
        out_specs=[col] * 4,
        out_shape=[out] * 4,
        scratch_shapes=[pltpu.VMEM((hpb, DH, DH), F32), pltpu.VMEM((hpb * SCAN_CB, CHUNK, DH), F32)],
        compiler_params=_cp("parallel", "arbitrary"),
    )(q, k, v, g, do, states)


W_IN_GRAD_ROWS = ((0, 640), (640, 736), (736, 1024))
W_IN_REF = 6688


def _layout_w_in(w):
    return jnp.pad(w, ((0, 0), (0, W_IN_COLS - W_IN_REF)))


def _unlayout_w_in(d):
    return d[:, :W_IN_REF]


def _gate_cols(w):
    return jnp.pad(w, ((0, 0), (GOFF, GW - GOFF - D)))


def _gate_rows(w):
    return jnp.pad(w, ((GOFF, GW - GOFF - D), (0, 0)))


def _layout_wgk(w):
    r = w.shape[1]
    top = jnp.concatenate([w[0], jnp.zeros_like(w[0])], axis=1)
    bot = jnp.concatenate([jnp.zeros_like(w[1]), w[1]], axis=1)
    return jnp.concatenate([top, bot, jnp.zeros((DH - 2 * r, D), w.dtype)], axis=0)


def _unlayout_wgk(d, r=16):
    return jnp.stack([d[:r, :HW], d[r:2 * r, HW:]])


def _local_step(z, target, modc, modx, norms, onw, hg_lb, wgk, bgk, w_in, get_mix, get_ffn, send):
    n_pre1, n_post1, n_pre2, n_post2 = norms
    t = z.shape[0]
    tm = 768 if t % 768 == 0 else 256
    h1 = _prenorm(z, n_pre1, modc, modx, 0, 1, "prenorm1")
    p = _matmul(h1, w_in, NN, BF16, "mm_in", tm, 512, D)
    q, v, k_f, k_b, g_f, g_b = _gates_fwd(p, hg_lb, wgk, bgk)
    o_f, st_f = _scan_fwd(q, k_f, v, g_f, False)
    o_b, st_b = _scan_fwd(q, k_b, v, g_b, True)
    y = _post_fwd(o_f, o_b, p, onw)
    w_br_hg, w_br_gla, w_out = get_mix(y)
    u1 = _matmul(y, w_br_hg, NN, BF16, "mm_br_hg", tm, GW, HW, a_off=0)
    u2 = _matmul(y, w_br_gla, NN, BF16, "mm_br_gla", tm, GW, HW, a_off=1)
    merged = _merge_fwd(p, u1, u2)
    y1 = _matmul(merged, w_out, NN, F32, "mm_out", tm, 512, GW)
    z1, h2 = _mid_fwd(z, y1, n_post1, n_pre2, modc, modx)
    w_gu_t, w_down = get_ffn(h2)
    uv = _matmul(h2, w_gu_t, NT, BF16, "mm_gu", tm, 512, D)
    act = _swiglu_fwd(uv)
    y2 = _matmul(act, w_down, NN, F32, "mm_down", tm, 512, D_FF // 2)
    dz, dy2, loss_vec, sm_final = _final(z1, y2, target, n_post2, modc, modx)
    dact = _matmul(dy2, w_down, NT, BF16, "mm_down_dx", tm, D_FF // 2, D)
    d_w_down = _matmul(act, dy2, TN, BF16, "mm_down_dw", D_FF // 2, 512, t)
    duv = _swiglu_bwd(uv, dact)
    dh2 = _matmul(duv, w_gu_t, NN, F32, "mm_gu_dx", tm, 512, D_FF // 2)
    d_w_gu_t = _matmul(duv, h2, TN, BF16, "mm_gu_dw", 512, 512, t)
    dh2 = send(("w_down", "w_gu_t"), (d_w_down, d_w_gu_t), dh2)
    dz, dy1, sm_mid = _mid_bwd(dh2, dz, z, z1, y1, n_post1, n_pre2, modc, modx)
    dmerged = _matmul(dy1, w_out, NT, F32, "mm_out_dx", tm, GW, D)
    d_w_out = _matmul(merged, dy1, TN, BF16, "mm_out_dw", GW, 512, t)
    du1, du2, dgm = _merge_bwd(dmerged, p, u1, u2)
    dy_hg = _matmul(du1, w_br_hg, NT, F32, "mm_br_hg_dx", tm, HW, GW)
    dy_gla = _matmul(du2, w_br_gla, NT, F32, "mm_br_gla_dx", tm, HW, GW)
    d_w_br_hg = _matmul(y, du1, TN, BF16, "mm_br_hg_dw", HW, GW, t, a_off=0, m_out=HW)
    d_w_br_gla = _matmul(y, du2, TN, BF16, "mm_br_gla_dw", HW, GW, t, a_off=1, m_out=HW)
    dy_hg = send(("w_out", "w_br_hg", "w_br_gla"), (d_w_out, d_w_br_hg, d_w_br_gla), dy_hg)
    do, dgo, sm_post = _post_bwd(dy_hg, dy_gla, o_f, o_b, p, onw)
    dq_f, dk_f, dv_f, dg_f = _scan_bwd(q, k_f, v, g_f, do, st_f, False)
    dq_b, dk_b, dv_b, dg_b = _scan_bwd(q, k_b, v, g_b, do, st_b, True)
    dp, d_lb, d_wgk, d_bgk = _gates_bwd(p, hg_lb, wgk, bgk, dgm, dgo, dq_f, dq_b, dv_f, dv_b, dk_f, dk_b, dg_f, dg_b)
    d_w_in = _matmul(h1, dp, TN, BF16, "mm_in_dw", 512, 512, t)
    dp = send(("w_in",), (d_w_in,), dp)
    dh1 = _matmul(dp, w_in, NT, F32, "mm_in_dx", tm, 512, 1024)
    grad_x, sm_pre = _pre_bwd(dh1, dz, z, n_pre1, modc, modx)
    return dict(loss_vec=loss_vec, grad_x=grad_x, sm_final=sm_final, sm_mid=sm_mid, sm_post=sm_post, sm_pre=sm_pre,
                d_lb=d_lb, d_wgk=d_wgk, d_bgk=d_bgk)


MESH = pl.DeviceIdType.MESH
ANY = pl.BlockSpec(memory_space=pl.ANY)
N_REL = N_DEV - 1


def _place():
    return lax.axis_index("x"), lax.axis_index("y"), lax.axis_index("c")


def _slot(p):
    return 4 * p[0] + 2 * p[1] + p[2]


def _all_gather(arrays, name):
    n = len(arrays)

    def body(*refs):
        ins, outs = refs[:n], refs[n:2 * n]
        send_sems, recv_sems, local_sems = refs[2 * n:]
        x, y, c = _place()
        me, sibling = (x, y, c), (x, y, 1 - c)
        chips = [(1 - x, y), (x, 1 - y), (1 - x, 1 - y)]

        def copy(a, k, block, to, src=None):
            dst = outs[a].at[_slot(block)]
            return pltpu.make_async_remote_copy(
                src_ref=dst if src is None else src, dst_ref=dst,
                send_sem=send_sems.at[N_REL * a + k], recv_sem=recv_sems.at[N_REL * a + k],
                device_id=to, device_id_type=MESH)

        mine = [pltpu.make_async_copy(ins[a], outs[a].at[_slot(me)], local_sems.at[a]) for a in range(n)]
        for cp in mine:
            cp.start()
        first = []
        for a in range(n):
            first.append(copy(a, 0, me, sibling, src=ins[a]))
            first += [copy(a, 1 + j, me, (*chip, c), src=ins[a]) for j, chip in enumerate(chips)]
        for cp in first:
            cp.start()
        passed = []
        for j, chip in enumerate(chips):
            for a in range(n):
                copy(a, 1 + j, (*chip, c), me).wait_recv()
                fwd = copy(a, 4 + j, (*chip, c), sibling)
                fwd.start()
                passed.append(fwd)
        for a in range(n):
            copy(a, 0, sibling, me).wait_recv()
        for j, chip in enumerate(chips):
            for a in range(n):
                copy(a, 4 + j, (*chip, 1 - c), me).wait_recv()
        for cp in first + passed:
            cp.wait_send()
        for cp in mine:
            cp.wait()

    return pl.pallas_call(
        body, name=name,
        in_specs=[ANY] * n, out_specs=[ANY] * n,
        out_shape=[jax.ShapeDtypeStruct((N_DEV,) + a.shape, a.dtype) for a in arrays],
        scratch_shapes=[pltpu.SemaphoreType.DMA((N_REL * n,)), pltpu.SemaphoreType.DMA((N_REL * n,)),
                        pltpu.SemaphoreType.DMA((n,))],
    )(*arrays)


def _exchange(arrays, name):
    n = len(arrays)

    def body(*refs):
        ins, outs = refs[:n], refs[n:2 * n]
        send_sems, recv_sems, local_sems = refs[2 * n:]
        x, y, c = _place()
        me = _slot((x, y, c))
        mine = [pltpu.make_async_copy(ins[a].at[me], outs[a].at[me], local_sems.at[a]) for a in range(n)]
        for cp in mine:
            cp.start()
        copies = []
        for a in range(n):
            for k in range(1, N_DEV):
                flip = lambda v, bit: 1 - v if bit else v
                peer = (flip(x, k & 4), flip(y, k & 2), flip(c, k & 1))
                copies.append(pltpu.make_async_remote_copy(
                    src_ref=ins[a].at[_slot(peer)], dst_ref=outs[a].at[me],
                    send_sem=send_sems.at[N_REL * a + k - 1], recv_sem=recv_sems.at[N_REL * a + k - 1],
                    device_id=peer, device_id_type=MESH))
                copies[-1].start()
        i = 0
        for a in range(n):
            for k in range(1, N_DEV):
                flip = lambda v, bit: 1 - v if bit else v
                peer = (flip(x, k & 4), flip(y, k & 2), flip(c, k & 1))
                pltpu.make_async_remote_copy(
                    src_ref=ins[a].at[_slot(peer)], dst_ref=outs[a].at[_slot(peer)],
                    send_sem=send_sems.at[N_REL * a + k - 1], recv_sem=recv_sems.at[N_REL * a + k - 1],
                    device_id=peer, device_id_type=MESH).wait_recv()
                i += 1
        for cp in copies:
            cp.wait_send()
        for cp in mine:
            cp.wait()

    return pl.pallas_call(
        body, name=name,
        in_specs=[ANY] * n, out_specs=[ANY] * n,
        out_shape=[jax.ShapeDtypeStruct(a.shape, a.dtype) for a in arrays],
        scratch_shapes=[pltpu.SemaphoreType.DMA((N_REL * n,)), pltpu.SemaphoreType.DMA((N_REL * n,)),
                        pltpu.SemaphoreType.DMA((n,))],
    )(*arrays)


HBM = pl.BlockSpec(memory_space=pltpu.HBM)
SEM = pl.BlockSpec(memory_space=pltpu.SEMAPHORE)
EFFECT = pltpu.SideEffectType.DATAFLOW_SIDE_EFFECTING


def _peer_of(x, y, c, k):
    flip = lambda v, bit: 1 - v if bit else v
    return flip(x, k & 4), flip(y, k & 2), flip(c, k & 1)


def _split_copies(gather, srcs, lands, send_sems, recv_sems, local_sems):
    x, y, c = _place()
    me = _slot((x, y, c))
    local, sends, waits = [], [], []
    for a, (src, land) in enumerate(zip(srcs, lands)):
        local.append(pltpu.make_async_copy(src if gather else src.at[me], land.at[me], local_sems.at[a]))
        for k in range(1, N_DEV):
            peer = _peer_of(x, y, c, k)
            mine = src if gather else src.at[_slot(peer)]
            sems = dict(send_sem=send_sems.at[N_REL * a + k - 1], recv_sem=recv_sems.at[N_REL * a + k - 1],
                        device_id=peer, device_id_type=MESH)
            sends.append(pltpu.make_async_remote_copy(src_ref=mine, dst_ref=land.at[me], **sems))
            waits.append(pltpu.make_async_remote_copy(src_ref=mine, dst_ref=land.at[_slot(peer)], **sems))
    return local, sends, waits


def _split_start(gather, srcs, name, after):
    n = len(srcs)
    land_shapes = [((N_DEV,) + s.shape) if gather else s.shape for s in srcs]
    lands = [lax.empty(shp, s.dtype) for shp, s in zip(land_shapes, srcs)]

    def body(*refs):
        src_refs, land_refs = refs[:n], refs[n:2 * n]
        send_sems, recv_sems, local_sems = refs[2 * n + 1:2 * n + 4]
        token = refs[-1]
        local, sends, _ = _split_copies(gather, src_refs, land_refs, send_sems, recv_sems, local_sems)
        for cp in local + sends:
            cp.start()
        token[...] = jnp.zeros_like(token)

    hbm = lambda a: pltpu.with_memory_space_constraint(a, pltpu.HBM)
    out = pl.pallas_call(
        body, name=name,
        out_shape=(pltpu.SemaphoreType.DMA((N_REL * n,)), pltpu.SemaphoreType.DMA((N_REL * n,)),
                   pltpu.SemaphoreType.DMA((n,)),
                   *[pltpu.HBM(s.shape, s.dtype) for s in srcs], *[pltpu.HBM(l.shape, l.dtype) for l in lands],
                   jax.ShapeDtypeStruct((8, DH), F32)),
        in_specs=[HBM] * (2 * n) + [ANY],
        out_specs=(SEM, SEM, SEM, *([HBM] * (2 * n)), pl.BlockSpec(memory_space=pltpu.VMEM)),
        input_output_aliases={i: 3 + i for i in range(2 * n)},
        compiler_params=pltpu.CompilerParams(has_side_effects=EFFECT),
    )(*[hbm(s) for s in srcs], *[hbm(l) for l in lands], after)
    return (gather, n, out[:3], out[3:3 + n], out[3 + n:3 + 2 * n]), out[-1]


def _split_wait(handle, name, after):
    gather, n, sems, srcs, lands = handle

    def body(*refs):
        src_refs, land_refs = refs[:n], refs[n:2 * n]
        send_sems, recv_sems, local_sems = refs[2 * n:2 * n + 3]
        local, _, waits = _split_copies(gather, src_refs, land_refs, send_sems, recv_sems, local_sems)
        for cp in waits:
            cp.wait_send()
            cp.wait_recv()
        for cp in local:
            cp.wait()

    out = pl.pallas_call(
        body, name=name,
        out_shape=(*[pltpu.HBM(s.shape, s.dtype) for s in srcs], *[pltpu.HBM(l.shape, l.dtype) for l in lands]),
        in_specs=[HBM] * (2 * n) + [SEM, SEM, SEM, ANY],
        out_specs=tuple([HBM] * (2 * n)),
        input_output_aliases={i: i for i in range(2 * n)},
        compiler_params=pltpu.CompilerParams(has_side_effects=EFFECT),
    )(*srcs, *lands, *sems, after)
    return list(out[n:])


def _tie(x, token, name):
    def body(x_ref, t_ref, o_ref):
        pass

    return pl.pallas_call(
        body, name=name, out_shape=jax.ShapeDtypeStruct(x.shape, x.dtype),
        in_specs=[ANY, ANY], out_specs=ANY, input_output_aliases={0: 0},
    )(x, token)


def _mod_fwd(a, w, b):
    def body(a_ref, w_ref, b_ref, o_ref):
        o_ref[...] = _dot(_silu(a_ref[...]), w_ref[...], NN, precision=HI) + b_ref[...]

    return pl.pallas_call(
        body, name="mod_fwd", out_shape=jax.ShapeDtypeStruct((a.shape[0], w.shape[1]), F32),
        compiler_params=pltpu.CompilerParams(vmem_limit_bytes=VMEM_LIMIT),
    )(a, w, b)


def _mod_bwd(a, d, w):
    def body(a_ref, d_ref, w_ref, dw_ref, dc_ref):
        av = a_ref[...]
        dv = d_ref[...]
        dw_ref[...] = _dot(_silu(av), dv, TN, precision=HI)
        da = _dot(dv[0:8, :], w_ref[...], NT, precision=HI) * _dsilu(av[0:8, :])
        row = lax.broadcasted_iota(jnp.int32, da.shape, 0)
        dc_ref[...] = jnp.where(row == 0, da, 0.0)

    return pl.pallas_call(
        body, name="mod_bwd",
        out_shape=[jax.ShapeDtypeStruct(w.shape, F32), jax.ShapeDtypeStruct((8, w.shape[0]), F32)],
        compiler_params=pltpu.CompilerParams(vmem_limit_bytes=VMEM_LIMIT),
    )(a, d, w)


def _sum_devices(g):
    def body(g_ref, o_ref):
        acc = g_ref[0]
        for i in range(1, g.shape[0]):
            acc = acc + g_ref[i]
        o_ref[...] = acc

    return pl.pallas_call(body, name="sum_devices_%d" % g.shape[1],
                          out_shape=jax.ShapeDtypeStruct(g.shape[1:], F32))(g)


def _adam_rows(r, c, n):
    budget = 6 * 1024 * 1024
    best = None
    for tr in range(16, r + 1, 16):
        if r % tr == 0 and tr * c * (2 * n + 28) <= budget:
            best = tr
    return best if best is not None else r


def _adamw(g, w, m, v, name):
    n, r, c = g.shape
    tr = _adam_rows(r, c, n)
    bc1 = 1.0 - ADAM_B1 ** ADAM_STEP
    bc2 = 1.0 - ADAM_B2 ** ADAM_STEP

    def body(g_ref, w_ref, m_ref, v_ref, go_ref, d_ref, mo_ref, vo_ref):
        grad = g_ref[0].astype(F32)
        for i in range(1, n):
            grad = grad + g_ref[i].astype(F32)
        go_ref[...] = grad
        m_new = ADAM_B1 * m_ref[...] + (1.0 - ADAM_B1) * grad
        v_new = ADAM_B2 * v_ref[...] + (1.0 - ADAM_B2) * (grad * grad)
        mo_ref[...] = m_new
        vo_ref[...] = v_new
        d_ref[...] = -ADAM_LR * ((m_new / bc1) / (jnp.sqrt(v_new / bc2) + ADAM_EPS) + ADAM_WD * w_ref[...])

    blk = pl.BlockSpec((tr, c), lambda i: (i, 0))
    out = jax.ShapeDtypeStruct((r, c), F32)
    return pl.pallas_call(
        body, name=name, grid=(r // tr,),
        in_specs=[pl.BlockSpec((n, tr, c), lambda i: (0, i, 0)), blk, blk, blk],
        out_specs=[blk] * 4, out_shape=[out] * 4,
        compiler_params=_cp("parallel"),
    )(g, w, m, v)


def kernel(x, c, ctx, c_ctx, w_mod, b_mod, norm_pre1, norm_post1, norm_pre2, norm_post2, w_in, hg_lb, hg_onorm, gla_w_gk, gla_b_gk, gla_onorm, w_br_hg, w_br_gla, w_out, w_ff_gate, w_ff_up, w_ff_down, loss_target, m_c_ctx, m_w_mod, m_b_mod, m_norm_pre1, m_norm_post1, m_norm_pre2, m_norm_post2, m_w_in, m_hg_lb, m_hg_onorm, m_gla_w_gk, m_gla_b_gk, m_gla_onorm, m_w_br_hg, m_w_br_gla, m_w_out, m_w_ff_gate, m_w_ff_up, m_w_ff_down, v_c_ctx, v_w_mod, v_b_mod, v_norm_pre1, v_norm_post1, v_norm_pre2, v_norm_post2, v_w_in, v_hg_lb, v_hg_onorm, v_gla_w_gk, v_gla_b_gk, v_gla_onorm, v_w_br_hg, v_w_br_gla, v_w_out, v_w_ff_gate, v_w_ff_up, v_w_ff_down):
    xi, yi, ci = lax.axis_index("x"), lax.axis_index("y"), lax.axis_index("c")
    me = 4 * xi + 2 * yi + ci
    t = CTX + x.shape[1]

    c_all, lb_g, wgk_g, bgk_g = _all_gather([c, hg_lb, gla_w_gk[0], gla_b_gk[0]], "ag_small")
    tr_ = lambda a: jnp.swapaxes(a[0], 0, 1)
    big = [w_in[0], w_br_hg[0], w_br_gla[0], w_out[0], tr_(w_ff_gate), tr_(w_ff_up), w_ff_down[0]]
    big_bf = [w.astype(BF16) for w in big]
    g_in, = _all_gather(big_bf[:1], "ag_w_in")
    mix_handle, tok = _split_start(True, big_bf[1:4], "ag_mix_start", g_in)
    ffn_handle, tok = _split_start(True, big_bf[4:], "ag_ffn_start", tok)
    cols = lambda g: jnp.transpose(g, (1, 0, 2)).reshape(g.shape[1], N_DEV * g.shape[2])
    w_in_k = _tie(_layout_w_in(cols(g_in)), tok, "tie_w_in")

    def get_mix(after):
        g_brh, g_brg, g_out = _split_wait(mix_handle, "ag_mix_wait", after)
        return _gate_cols(cols(g_brh)), _gate_cols(cols(g_brg)), _gate_rows(g_out.reshape(D, D))

    def get_ffn(after):
        g_gate, g_up, g_down = _split_wait(ffn_handle, "ag_ffn_wait", after)
        return jnp.concatenate([g_gate.reshape(D_FF, D), g_up.reshape(D_FF, D)], axis=0), g_down.reshape(D_FF, D)

    hg_lb_full = jnp.transpose(lb_g, (1, 2, 0, 3)).reshape(2, 2, HW)
    wgk_k = _layout_wgk(jnp.transpose(wgk_g, (1, 2, 0, 3)).reshape(2, 16, HW)).astype(BF16)
    bgk_k = jnp.transpose(bgk_g, (1, 0, 2)).reshape(1, D)
    onw = jnp.concatenate([jnp.tile(hg_onorm, (1, NH // 2)), jnp.tile(gla_onorm, (1, NH // 2))], axis=1)

    n_mod = w_mod.shape[2]
    a9 = jnp.concatenate([c_ctx[None], c_all[:, 0], jnp.zeros((16 - 1 - N_DEV, D), F32)], axis=0)
    b_loc = lax.dynamic_slice(b_mod, (0, me * n_mod), (1, n_mod))
    s_loc = _mod_fwd(a9, w_mod[0], b_loc)
    s_all, = _all_gather([s_loc], "ag_mod")
    mod_all = jnp.transpose(s_all, (1, 0, 2)).reshape(16, N_DEV * n_mod)
    pad8 = lambda m: jnp.concatenate([m.reshape(6, D), jnp.zeros((2, D), F32)], axis=0)
    modc = pad8(mod_all[0])
    modx = pad8(lax.dynamic_slice(mod_all, (1 + me, 0), (1, N_DEV * n_mod))[0])

    z = jnp.concatenate([ctx[0], x[0]], axis=0)
    norms = (norm_pre1, norm_post1, norm_pre2, norm_post2)
    shard = lambda d: jnp.transpose(d.reshape(d.shape[0], N_DEV, -1), (1, 0, 2)).astype(BF16)
    rowshard = lambda d: d.reshape(N_DEV, d.shape[0] // N_DEV, d.shape[1]).astype(BF16)
    sent, pending = [], []

    def send(names, grads, x_after):
        arrs, leaves = [], []
        for nm, g in zip(names, grads):
            if nm == "w_gu_t":
                arrs += [rowshard(g[:D_FF]), rowshard(g[D_FF:])]
                leaves += ["w_ff_gate", "w_ff_up"]
            elif nm == "w_in":
                full = shard(_unlayout_w_in(g))
                chunks = [full[:, a:b] for a, b in W_IN_GRAD_ROWS]
                arrs.append(chunks[0])
                leaves.append("w_in#0")
                pending.extend(chunks[1:])
            elif nm == "w_down":
                arrs.append(rowshard(g))
                leaves.append("w_ff_down")
            elif nm == "w_out":
                arrs.append(rowshard(g[GOFF:GOFF + D]))
                leaves.append(nm)
            else:
                arrs.append(shard(g[:, GOFF:GOFF + D]))
                leaves.append(nm)
        handle, tok = _split_start(False, arrs, "grads_%s_start" % names[0], x_after)
        sent.append((names[0], leaves, handle))
        return _tie(x_after, tok, "tie_" + names[0])

    def send_pending(i, x_after):
        handle, tok = _split_start(False, [pending[i - 1]], "grads_w_in%d_start" % i, x_after)
        sent.append(("w_in%d" % i, ["w_in#%d" % i], handle))
        return _tie(x_after, tok, "tie_w_in%d" % i)

    r = _local_step(z, loss_target[0], modc, modx, norms, onw, hg_lb_full, wgk_k, bgk_k,
                    w_in_k, get_mix, get_ffn, send)
    loss = lax.psum((0.5 / D) * jnp.sum(r["loss_vec"]), ("x", "y", "c"))
    grad_x = r["grad_x"][None]

    sm_pre, sm_mid, sm_fin = r["sm_pre"], r["sm_mid"], r["sm_final"]
    dmodc = jnp.stack([sm_pre[0], sm_pre[2], sm_mid[4], sm_mid[0], sm_mid[2], sm_fin[0]]).reshape(-1)
    dmodx = jnp.stack([sm_pre[1], sm_pre[3], sm_mid[5], sm_mid[1], sm_mid[3], sm_fin[1]]).reshape(-1)
    on = r["sm_post"][0].reshape(NH, DH)
    pieces = [dmodc, dmodx, sm_pre[4], sm_mid[7], sm_mid[6], sm_fin[2], on[:NH // 2].sum(0), on[NH // 2:].sum(0),
              r["d_lb"][:2].reshape(-1), _unlayout_wgk(r["d_wgk"]).reshape(-1), r["d_bgk"][0]]
    sizes = [p.shape[0] for p in pieces]
    pack = jnp.concatenate(pieces).reshape(-1, DH)
    pack_all, = _all_gather([pack], "ag_small_grads")
    pack_all = send_pending(1, pack_all)
    tot = _sum_devices(pack_all).reshape(-1)
    offs = [sum(sizes[:i]) for i in range(len(sizes))]
    part = lambda i: tot[offs[i]:offs[i] + sizes[i]]
    dmodc_t, dmodx_t = part(0), part(1)
    g_b_mod = (dmodc_t + dmodx_t)[None]
    g_norms = [part(i)[None] for i in (2, 3, 4, 5)]
    g_hg_on, g_gla_on = part(6)[None], part(7)[None]
    lb0 = lax.dynamic_slice(part(8).reshape(2, HW), (0, me * (HW // N_DEV)), (2, HW // N_DEV))
    g_hg_lb = jnp.stack([lb0, -lb0])
    g_wgk = lax.dynamic_slice(part(9).reshape(2, 16, HW), (0, 0, me * (HW // N_DEV)), (2, 16, HW // N_DEV))[None]
    g_bgk = lax.dynamic_slice(part(10).reshape(2, HW), (0, me * (HW // N_DEV)), (2, HW // N_DEV))[None]

    dmx_all = pack_all.reshape(N_DEV, -1)[:, sizes[0]:sizes[0] + sizes[1]]
    d9 = jnp.concatenate([lax.dynamic_slice(dmodc_t[None], (0, me * n_mod), (1, n_mod)),
                          lax.dynamic_slice(dmx_all, (0, me * n_mod), (N_DEV, n_mod)),
                          jnp.zeros((16 - 1 - N_DEV, n_mod), F32)], axis=0)
    g_w_mod, dcc_part = _mod_bwd(a9, d9, w_mod[0])
    dcc_all, = _all_gather([dcc_part], "ag_c_ctx")
    dcc_all = send_pending(2, dcc_all)
    g_c_ctx = _sum_devices(dcc_all)[0]

    recv = {}
    for first, leaves, handle in sent:
        if not first.startswith("w_in"):
            recv.update(zip(leaves, _split_wait(handle, "grads_%s_wait" % first, r["grad_x"])))
    moms = [(m_w_in, v_w_in), (m_w_br_hg, v_w_br_hg), (m_w_br_gla, v_w_br_gla), (m_w_out, v_w_out),
            (m_w_ff_gate, v_w_ff_gate), (m_w_ff_up, v_w_ff_up), (m_w_ff_down, v_w_ff_down)]
    names = ["w_in", "w_br_hg", "w_br_gla", "w_out", "w_ff_gate", "w_ff_up", "w_ff_down"]
    res = {}

    def update(nm, w, m, v):
        if nm in ("w_ff_gate", "w_ff_up"):
            outs = _adamw(recv[nm], w, tr_(m), tr_(v), "adamw_" + nm)
            res[nm] = [jnp.swapaxes(o, 0, 1)[None] for o in outs]
        else:
            res[nm] = [o[None] for o in _adamw(recv[nm], w, m[0], v[0], "adamw_" + nm)]

    for nm, w, (m, v) in list(zip(names, big, moms))[1:]:
        update(nm, w, m, v)
    res["w_mod"] = [o[None] for o in _adamw(g_w_mod[None], w_mod[0], m_w_mod[0], v_w_mod[0], "adamw_w_mod")]

    small = [("c_ctx", c_ctx, m_c_ctx, v_c_ctx, g_c_ctx), ("b_mod", b_mod, m_b_mod, v_b_mod, g_b_mod),
             ("norm_pre1", norm_pre1, m_norm_pre1, v_norm_pre1, g_norms[0]),
             ("norm_post1", norm_post1, m_norm_post1, v_norm_post1, g_norms[1]),
             ("norm_pre2", norm_pre2, m_norm_pre2, v_norm_pre2, g_norms[2]),
             ("norm_post2", norm_post2, m_norm_post2, v_norm_post2, g_norms[3]),
             ("hg_lb", hg_lb, m_hg_lb, v_hg_lb, g_hg_lb), ("hg_onorm", hg_onorm, m_hg_onorm, v_hg_onorm, g_hg_on),
             ("gla_w_gk", gla_w_gk, m_gla_w_gk, v_gla_w_gk, g_wgk), ("gla_b_gk", gla_b_gk, m_gla_b_gk, v_gla_b_gk, g_bgk),
             ("gla_onorm", gla_onorm, m_gla_onorm, v_gla_onorm, g_gla_on)]
    flat = lambda k: jnp.concatenate([s[k].reshape(-1) for s in small]).reshape(-1, DH)
    outs = _adamw(flat(4)[None], flat(1), flat(2), flat(3), "adamw_small")
    off = 0
    for nm, w, _, _, _ in small:
        res[nm] = [o.reshape(-1)[off:off + w.size].reshape(w.shape) for o in outs]
        off += w.size

    for first, leaves, handle in sent:
        if first.startswith("w_in"):
            recv.update(zip(leaves, _split_wait(handle, "grads_%s_wait" % first, outs[0])))
    recv["w_in"] = jnp.concatenate([recv.pop("w_in#%d" % i) for i in range(len(W_IN_GRAD_ROWS))], axis=1)
    update("w_in", big[0], *moms[0])

    order = ["c_ctx", "w_mod", "b_mod", "norm_pre1", "norm_post1", "norm_pre2", "norm_post2", "w_in", "hg_lb",
             "hg_onorm", "gla_w_gk", "gla_b_gk", "gla_onorm", "w_br_hg", "w_br_gla", "w_out", "w_ff_gate", "w_ff_up",
             "w_ff_down"]
    return (loss, grad_x, *[res[n][k] for k in range(4) for n in order])
```

```python
import functools

import jax
import jax.numpy as jnp
from jax import lax
from jax.experimental import pallas as pl
from jax.experimental.pallas import tpu as pltpu

F32 = jnp.float32
BF16 = jnp.bfloat16
HI = lax.Precision.HIGHEST

N_DEV = 8
D = 1024
CTX = 256
HW = 512
DH = 128
NH = 8
D_FF = 2816
EPS = 1e-6
GLA_NORM = 16.0
CHUNK = 64
TR = 256
NCT = CTX // TR
W_IN_COLS = 7168
MAIN0 = 0
LR0 = 4608
GW = 1152
GOFF = 32
GATE_HG0 = LR0
GATE_GLA0 = LR0 + D
LEVELS = (32, 16, 8)
EXP_CLAMP = 80.0
VMEM_LIMIT = 48 * 1024 * 1024

ADAM_LR, ADAM_B1, ADAM_B2, ADAM_EPS, ADAM_WD, ADAM_STEP = 0.001, 0.9, 0.999, 1e-08, 0.01, 10


def _cp(*sem):
    return pltpu.CompilerParams(dimension_semantics=sem, vmem_limit_bytes=VMEM_LIMIT)


def _sig(x):
    return jax.nn.sigmoid(x)


def _silu(x):
    return x * _sig(x)


def _dsilu(x):
    s = _sig(x)
    return s * (1.0 + x * (1.0 - s))


def _rstd(x):
    return lax.rsqrt(jnp.mean(x * x, axis=-1, keepdims=True) + EPS)


def _rms_bwd(a, y, r):
    return r * (a - y * (r * r) * jnp.mean(a * y, axis=-1, keepdims=True))


def _colsum(x):
    return jnp.sum(x, axis=0, keepdims=True)


def _dot(a, b, dims, precision=None):
    return lax.dot_general(a, b, (dims, ((), ())), preferred_element_type=F32, precision=precision)


NN = ((1,), (0,))
NT = ((1,), (1,))
TN = ((0,), (0,))

SCAN_HEADS_FWD = 4
SCAN_HEADS_BWD = 4


def _split_dot(m, x):
    mb = m.astype(BF16)
    x1 = x.astype(BF16)
    r1 = x - x1.astype(F32)
    x2 = r1.astype(BF16)
    x3 = (r1 - x2.astype(F32)).astype(BF16)
    return _dot(mb, x1, NN) + _dot(mb, x2, NN) + _dot(mb, x3, NN)


def _matmul(a, b, dims, out_dtype, name, tm, tn, tk, a_off=0, m_out=None):
    if dims == NN:
        m, k = a.shape[0], b.shape[0]
        n = b.shape[1]
        a_spec = pl.BlockSpec((tm, tk), lambda i, j, kk: (i, kk + a_off))
        b_spec = pl.BlockSpec((tk, tn), lambda i, j, kk: (kk, j))
    elif dims == NT:
        m, k = a.shape[0], b.shape[1]
        n = b.shape[0]
        a_spec = pl.BlockSpec((tm, tk), lambda i, j, kk: (i, kk + a_off))
        b_spec = pl.BlockSpec((tn, tk), lambda i, j, kk: (j, kk))
    else:
        m, k = (a.shape[1] if m_out is None else m_out), a.shape[0]
        n = b.shape[1]
        a_spec = pl.BlockSpec((tk, tm), lambda i, j, kk: (kk, i + a_off))
        b_spec = pl.BlockSpec((tk, tn), lambda i, j, kk: (kk, j))
    assert m % tm == 0 and n % tn == 0 and k % tk == 0, (name, m, n, k, tm, tn, tk)
    nk = k // tk

    def body(a_ref, b_ref, o_ref, *acc):
        part = _dot(a_ref[...], b_ref[...], dims)
        if nk == 1:
            o_ref[...] = part.astype(o_ref.dtype)
            return
        acc_ref, = acc
        kk = pl.program_id(2)

        @pl.when(kk == 0)
        def _():
            acc_ref[...] = part

        @pl.when(kk > 0)
        def _():
            acc_ref[...] += part

        @pl.when(kk == nk - 1)
        def _():
            o_ref[...] = acc_ref[...].astype(o_ref.dtype)

    return pl.pallas_call(
        body,
        name=name,
        grid=(m // tm, n // tn, nk),
        in_specs=[a_spec, b_spec],
        out_specs=pl.BlockSpec((tm, tn), lambda i, j, kk: (i, j)),
        out_shape=jax.ShapeDtypeStruct((m, n), out_dtype),
        scratch_shapes=[] if nk == 1 else [pltpu.VMEM((tm, tn), F32)],
        compiler_params=_cp("parallel", "parallel", "arbitrary"),
    )(a, b)


def _row(c):
    return pl.BlockSpec((TR, c), lambda i: (i, 0))


def _rowcol(width, cb):
    return pl.BlockSpec((TR, width), lambda i: (i, cb))


def _full(shape):
    return pl.BlockSpec(shape, lambda i: (0,) * len(shape))


def _mod_row(mc_ref, mx_ref, k, is_ctx):
    return jnp.where(is_ctx, mc_ref[k:k + 1, :], mx_ref[k:k + 1, :])


def _acc_row(ref, k, val):
    ref[k:k + 1, :] += val


def _acc_mod(ref, k, is_ctx, val):
    zero = jnp.zeros_like(val)
    ref[k:k + 1, :] += jnp.where(is_ctx, val, zero)
    ref[k + 1:k + 2, :] += jnp.where(is_ctx, zero, val)


def _prenorm(z, nw, modc, modx, i_shift, i_scale, name):
    t = z.shape[0]

    def body(z_ref, nw_ref, mc_ref, mx_ref, h_ref):
        is_ctx = pl.program_id(0) < NCT
        x = z_ref[...]
        n = x * _rstd(x) * nw_ref[...]
        h = n * (1.0 + _mod_row(mc_ref, mx_ref, i_scale, is_ctx)) + _mod_row(mc_ref, mx_ref, i_shift, is_ctx)
        h_ref[...] = h.astype(BF16)

    return pl.pallas_call(
        body, name=name, grid=(t // TR,),
        in_specs=[_row(D), _full((1, D)), _full((8, D)), _full((8, D))],
        out_specs=_row(D),
        out_shape=jax.ShapeDtypeStruct((t, D), BF16),
        compiler_params=_cp("parallel"),
    )(z, nw, modc, modx)


def _hg_lb(lb_ref, d):
    a0 = lb_ref[0, d:d + 1, :]
    a1 = lb_ref[1, d:d + 1, :]
    mx = jnp.maximum(a0, a1)
    e0 = jnp.exp(a0 - mx)
    e1 = jnp.exp(a1 - mx)
    return e0 / (e0 + e1)


def _log_sigmoid(x):
    return jnp.minimum(x, 0.0) - jnp.log(1.0 + jnp.exp(-jnp.abs(x)))


def _gates_fwd(p, hg_lb, wgk, bgk):
    t = p.shape[0]
    seg = lambda j: _rowcol(HW, MAIN0 // HW + j)

    def body(hq_ref, hi_ref, hf_ref, hb_ref, gq_ref, gk_ref, gv_ref, lr_ref, lb_ref, wgk_ref, bgk_ref,
             q_ref, v_ref, kf_ref, kb_ref, gf_ref, gb_ref):
        q_ref[:, :HW] = _silu(hq_ref[...].astype(F32))
        q_ref[:, HW:] = gq_ref[...].astype(F32) * (DH ** -0.5)
        v_ref[:, :HW] = hi_ref[...].astype(F32)
        v_ref[:, HW:] = gv_ref[...].astype(F32)
        xg = _dot(lr_ref[...].astype(BF16), wgk_ref[...], NN) + bgk_ref[...]
        for d, (raw_ref, k_ref, g_ref) in enumerate(((hf_ref, kf_ref, gf_ref), (hb_ref, kb_ref, gb_ref))):
            lbd = _hg_lb(lb_ref, d)
            f = lbd + (1.0 - lbd) * _sig(raw_ref[...].astype(F32))
            k_ref[:, :HW] = 1.0 - f
            k_ref[:, HW:] = gk_ref[...].astype(F32)
            g_ref[:, :HW] = jnp.log(f)
            g_ref[:, HW:] = _log_sigmoid(xg[:, d * HW:(d + 1) * HW]) * (1.0 / GLA_NORM)

    out = jax.ShapeDtypeStruct((t, D), F32)
    return pl.pallas_call(
        body, name="gates_fwd", grid=(t // TR,),
        in_specs=[seg(0), seg(1), seg(2), seg(3), seg(5), seg(6), seg(7), _rowcol(DH, LR0 // DH),
                  _full((2, 2, HW)), _full((DH, D)), _full((1, D))],
        out_specs=[_row(D)] * 6,
        out_shape=[out] * 6,
        compiler_params=_cp("parallel"),
    )(p, p, p, p, p, p, p, p, hg_lb, wgk, bgk)


def _post_fwd(o_fw, o_bw, p, onw):
    t = o_fw.shape[0]

    def body(of_ref, ob_ref, g1_ref, g2_ref, w_ref, y_ref):
        for h in range(NH):
            sl = slice(h * DH, (h + 1) * DH)
            o = of_ref[:, sl] + ob_ref[:, sl]
            g_ref = g1_ref if h < NH // 2 else g2_ref
            gs = slice((h % (NH // 2)) * DH, (h % (NH // 2) + 1) * DH)
            n = o * _rstd(o) * w_ref[:, sl]
            y_ref[:, sl] = (n * _silu(g_ref[:, gs].astype(F32))).astype(BF16)

    return pl.pallas_call(
        body, name="post_fwd", grid=(t // TR,),
        in_specs=[_row(D), _row(D), _rowcol(HW, MAIN0 // HW + 4), _rowcol(HW, MAIN0 // HW + 8), _full((1, D))],
        out_specs=_row(D),
        out_shape=jax.ShapeDtypeStruct((t, D), BF16),
        compiler_params=_cp("parallel"),
    )(o_fw, o_bw, p, p, onw)


def _gate_window_specs(col0):
    return [_rowcol(HW, col0 // HW), _rowcol(HW, col0 // HW + 1), _rowcol(DH, (col0 + 2 * HW) // DH)]


def _gate_window(refs):
    return jnp.concatenate([r[...].astype(F32) for r in refs], axis=1)


def _merge_fwd(p, u1, u2):
    t = p.shape[0]

    def body(a0, a1, a2, b0, b1, b2, u1_ref, u2_ref, m_ref):
        f = lambda r: r[...].astype(F32)
        m_ref[...] = (_sig(_gate_window((a0, a1, a2))) * f(u1_ref)
                      + _sig(_gate_window((b0, b1, b2))) * f(u2_ref)).astype(BF16)

    return pl.pallas_call(
        body, name="merge_fwd", grid=(t // TR,),
        in_specs=_gate_window_specs(GATE_HG0) + _gate_window_specs(GATE_GLA0) + [_row(GW), _row(GW)],
        out_specs=_row(GW),
        out_shape=jax.ShapeDtypeStruct((t, GW), BF16),
        compiler_params=_cp("parallel"),
    )(p, p, p, p, p, p, u1, u2)


def _mid_fwd(z, y1, nw_post, nw_pre, modc, modx):
    t = z.shape[0]

    def body(z_ref, y_ref, wpo_ref, wpr_ref, mc_ref, mx_ref, z1_ref, h_ref):
        is_ctx = pl.program_id(0) < NCT
        y = y_ref[...]
        z1 = z_ref[...] + _mod_row(mc_ref, mx_ref, 2, is_ctx) * (y * _rstd(y) * wpo_ref[...])
        z1_ref[...] = z1
        n = z1 * _rstd(z1) * wpr_ref[...]
        h = n * (1.0 + _mod_row(mc_ref, mx_ref, 4, is_ctx)) + _mod_row(mc_ref, mx_ref, 3, is_ctx)
        h_ref[...] = h.astype(BF16)

    return pl.pallas_call(
        body, name="mid_fwd", grid=(t // TR,),
        in_specs=[_row(D), _row(D), _full((1, D)), _full((1, D)), _full((8, D)), _full((8, D))],
        out_specs=[_row(D), _row(D)],
        out_shape=[jax.ShapeDtypeStruct((t, D), F32), jax.ShapeDtypeStruct((t, D), BF16)],
        compiler_params=_cp("parallel"),
    )(z, y1, nw_post, nw_pre, modc, modx)


def _swiglu_fwd(uv):
    t = uv.shape[0]

    def body(u_ref, v_ref, a_ref):
        a_ref[...] = (_silu(u_ref[...].astype(F32)) * v_ref[...].astype(F32)).astype(BF16)

    return pl.pallas_call(
        body, name="swiglu_fwd", grid=(t // TR,),
        in_specs=[_rowcol(D_FF, 0), _rowcol(D_FF, 1)],
        out_specs=_row(D_FF),
        out_shape=jax.ShapeDtypeStruct((t, D_FF), BF16),
        compiler_params=_cp("parallel"),
    )(uv, uv)


def _swiglu_bwd(uv, da):
    t = uv.shape[0]

    def body(u_ref, v_ref, da_ref, d_ref):
        u = u_ref[...].astype(F32)
        d = da_ref[...].astype(F32)
        d_ref[:, :D_FF] = (d * v_ref[...].astype(F32) * _dsilu(u)).astype(BF16)
        d_ref[:, D_FF:] = (d * _silu(u)).astype(BF16)

    return pl.pallas_call(
        body, name="swiglu_bwd", grid=(t // TR,),
        in_specs=[_rowcol(D_FF, 0), _rowcol(D_FF, 1), _row(D_FF)],
        out_specs=_row(2 * D_FF),
        out_shape=jax.ShapeDtypeStruct((t, 2 * D_FF), BF16),
        compiler_params=_cp("parallel"),
    )(uv, uv, da)


def _final(z1, y2, target, nw, modc, modx):
    t = z1.shape[0]

    def body(z1_ref, y_ref, tg_ref, w_ref, mc_ref, mx_ref, dz_ref, dy_ref, loss_ref, sm_ref):
        i = pl.program_id(0)
        is_ctx = i < NCT

        @pl.when(i == 0)
        def _():
            loss_ref[...] = jnp.zeros_like(loss_ref)
            sm_ref[...] = jnp.zeros_like(sm_ref)

        g = _mod_row(mc_ref, mx_ref, 5, is_ctx)
        y = y_ref[...]
        r = _rstd(y)
        w = w_ref[...]
        yr = y * r
        n = yr * w
        e = z1_ref[...] + g * n - tg_ref[...]
        lat = jnp.where(is_ctx, 0.0, 1.0)
        loss_ref[...] += lat * _colsum(e * e)
        dz = e * (lat / D)
        dz_ref[...] = dz
        _acc_mod(sm_ref, 0, is_ctx, _colsum(dz * n))
        dn = dz * g
        _acc_row(sm_ref, 2, _colsum(dn * yr))
        dy_ref[...] = _rms_bwd(dn * w, y, r).astype(BF16)

    return pl.pallas_call(
        body, name="final", grid=(t // TR,),
        in_specs=[_row(D), _row(D), pl.BlockSpec((TR, D), lambda i: (jnp.maximum(i - NCT, 0), 0)),
                  _full((1, D)), _full((8, D)), _full((8, D))],
        out_specs=[_row(D), _row(D), _full((1, D)), _full((8, D))],
        out_shape=[jax.ShapeDtypeStruct((t, D), F32), jax.ShapeDtypeStruct((t, D), BF16),
                   jax.ShapeDtypeStruct((1, D), F32), jax.ShapeDtypeStruct((8, D), F32)],
        compiler_params=_cp("arbitrary"),
    )(z1, y2, target, nw, modc, modx)


def _mid_bwd(dh2, dz, z, z1, y1, nw_post, nw_pre, modc, modx):
    t = z.shape[0]

    def body(dh_ref, dz_ref, z_ref, z1_ref, y_ref, wpo_ref, wpr_ref, mc_ref, mx_ref, dzo_ref, dy_ref, sm_ref):
        i = pl.program_id(0)
        is_ctx = i < NCT

        @pl.when(i == 0)
        def _():
            sm_ref[...] = jnp.zeros_like(sm_ref)

        dh = dh_ref[...]
        z1 = z1_ref[...]
        r = _rstd(z1)
        zr = z1 * r
        wpr = wpr_ref[...]
        n = zr * wpr
        _acc_mod(sm_ref, 0, is_ctx, _colsum(dh))
        _acc_mod(sm_ref, 2, is_ctx, _colsum(dh * n))
        dn = dh * (1.0 + _mod_row(mc_ref, mx_ref, 4, is_ctx))
        _acc_row(sm_ref, 6, _colsum(dn * zr))
        dz1 = dz_ref[...] + _rms_bwd(dn * wpr, z1, r)
        dzo_ref[...] = dz1
        y = y_ref[...]
        r1 = _rstd(y)
        yr = y * r1
        wpo = wpo_ref[...]
        g = _mod_row(mc_ref, mx_ref, 2, is_ctx)
        _acc_mod(sm_ref, 4, is_ctx, _colsum(dz1 * (yr * wpo)))
        dn1 = dz1 * g
        _acc_row(sm_ref, 7, _colsum(dn1 * yr))
        dy_ref[...] = _rms_bwd(dn1 * wpo, y, r1).astype(BF16)

    return pl.pallas_call(
        body, name="mid_bwd", grid=(t // TR,),
        in_specs=[_row(D)] * 5 + [_full((1, D)), _full((1, D)), _full((8, D)), _full((8, D))],
        out_specs=[_row(D), _row(D), _full((8, D))],
        out_shape=[jax.ShapeDtypeStruct((t, D), F32), jax.ShapeDtypeStruct((t, D), BF16),
                   jax.ShapeDtypeStruct((8, D), F32)],
        compiler_params=_cp("arbitrary"),
    )(dh2, dz, z, z1, y1, nw_post, nw_pre, modc, modx)


def _pre_bwd(dh1, dz, z, nw, modc, modx):
    t = z.shape[0]

    def body(dh_ref, dz_ref, z_ref, w_ref, mc_ref, mx_ref, dzo_ref, sm_ref):
        i = pl.program_id(0)
        is_ctx = i < NCT

        @pl.when(i == 0)
        def _():
            sm_ref[...] = jnp.zeros_like(sm_ref)

        dh = dh_ref[...]
        x = z_ref[...]
        r = _rstd(x)
        xr = x * r
        w = w_ref[...]
        _acc_mod(sm_ref, 0, is_ctx, _colsum(dh))
        _acc_mod(sm_ref, 2, is_ctx, _colsum(dh * (xr * w)))
        dn = dh * (1.0 + _mod_row(mc_ref, mx_ref, 1, is_ctx))
        _acc_row(sm_ref, 4, _colsum(dn * xr))
        dzo_ref[...] = dz_ref[...] + _rms_bwd(dn * w, x, r)

    return pl.pallas_call(
        body, name="pre_bwd", grid=(t // TR,),
        in_specs=[_row(D)] * 3 + [_full((1, D)), _full((8, D)), _full((8, D))],
        out_specs=[pl.BlockSpec((TR, D), lambda i: (jnp.maximum(i - NCT, 0), 0)), _full((8, D))],
        out_shape=[jax.ShapeDtypeStruct((t - CTX, D), F32), jax.ShapeDtypeStruct((8, D), F32)],
        compiler_params=_cp("arbitrary"),
    )(dh1, dz, z, nw, modc, modx)


def _merge_bwd(dm, p, u1, u2):
    t = dm.shape[0]

    def body(dm_ref, a0, a1, a2, b0, b1, b2, u1_ref, u2_ref, du1_ref, du2_ref, dg_ref):
        dm_ = dm_ref[...]
        s1 = _sig(_gate_window((a0, a1, a2)))
        s2 = _sig(_gate_window((b0, b1, b2)))
        du1_ref[...] = (dm_ * s1).astype(BF16)
        du2_ref[...] = (dm_ * s2).astype(BF16)
        dg_ref[:, :GW] = (dm_ * u1_ref[...].astype(F32) * s1 * (1.0 - s1)).astype(BF16)
        dg_ref[:, GW:] = (dm_ * u2_ref[...].astype(F32) * s2 * (1.0 - s2)).astype(BF16)

    return pl.pallas_call(
        body, name="merge_bwd", grid=(t // TR,),
        in_specs=[_row(GW)] + _gate_window_specs(GATE_HG0) + _gate_window_specs(GATE_GLA0) + [_row(GW), _row(GW)],
        out_specs=[_row(GW), _row(GW), _row(2 * GW)],
        out_shape=[jax.ShapeDtypeStruct((t, GW), BF16), jax.ShapeDtypeStruct((t, GW), BF16),
                   jax.ShapeDtypeStruct((t, 2 * GW), BF16)],
        compiler_params=_cp("parallel"),
    )(dm, p, p, p, p, p, p, u1, u2)


def _post_bwd(dy_hg, dy_gla, o_fw, o_bw, p, onw):
    t = o_fw.shape[0]

    def body(d1_ref, d2_ref, of_ref, ob_ref, g1_ref, g2_ref, w_ref, do_ref, dg_ref, sm_ref):
        @pl.when(pl.program_id(0) == 0)
        def _():
            sm_ref[...] = jnp.zeros_like(sm_ref)

        for h in range(NH):
            sl = slice(h * DH, (h + 1) * DH)
            gs = slice((h % (NH // 2)) * DH, (h % (NH // 2) + 1) * DH)
            g_ref, d_ref = (g1_ref, d1_ref) if h < NH // 2 else (g2_ref, d2_ref)
            o = of_ref[:, sl] + ob_ref[:, sl]
            r = _rstd(o)
            orr = o * r
            w = w_ref[:, sl]
            gt = g_ref[:, gs].astype(F32)
            dy = d_ref[:, gs]
            dg_ref[:, sl] = (dy * (orr * w) * _dsilu(gt)).astype(BF16)
            dn = dy * _silu(gt)
            sm_ref[0:1, sl] += _colsum(dn * orr)
            do_ref[:, sl] = _rms_bwd(dn * w, o, r)

    return pl.pallas_call(
        body, name="post_bwd", grid=(t // TR,),
        in_specs=[_row(HW), _row(HW), _row(D), _row(D), _rowcol(HW, MAIN0 // HW + 4), _rowcol(HW, MAIN0 // HW + 8),
                  _full((1, D))],
        out_specs=[_row(D), _row(D), _full((8, D))],
        out_shape=[jax.ShapeDtypeStruct((t, D), F32), jax.ShapeDtypeStruct((t, D), BF16),
                   jax.ShapeDtypeStruct((8, D), F32)],
        compiler_params=_cp("arbitrary"),
    )(dy_hg, dy_gla, o_fw, o_bw, p, p, onw)


def _gates_bwd(p, hg_lb, wgk, bgk, dgm, dgo, dq_f, dq_b, dv_f, dv_b, dk_f, dk_b, dg_f, dg_b):
    t = p.shape[0]
    seg = lambda j: _rowcol(HW, MAIN0 // HW + j)

    def body(hq_ref, hf_ref, hb_ref, lr_ref, lb_ref, wgk_ref, bgk_ref, dgm_ref, dgo_ref,
             dqf_ref, dqb_ref, dvf_ref, dvb_ref, dkf_ref, dkb_ref, dgf_ref, dgb_ref,
             dp_ref, dlb_ref, dw_ref, db_ref):
        @pl.when(pl.program_id(0) == 0)
        def _():
            dlb_ref[...] = jnp.zeros_like(dlb_ref)
            dw_ref[...] = jnp.zeros_like(dw_ref)
            db_ref[...] = jnp.zeros_like(db_ref)

        c0 = MAIN0

        def put(j, val):
            dp_ref[:, c0 + j * HW:c0 + (j + 1) * HW] = val.astype(BF16)

        dq = dqf_ref[...] + dqb_ref[...]
        dv = dvf_ref[...] + dvb_ref[...]
        put(0, dq[:, :HW] * _dsilu(hq_ref[...].astype(F32)))
        put(1, dv[:, :HW])
        put(5, dq[:, HW:] * (DH ** -0.5))
        put(7, dv[:, HW:])
        put(6, dkf_ref[:, HW:] + dkb_ref[:, HW:])
        dp_ref[:, c0 + 4 * HW:c0 + 5 * HW] = dgo_ref[:, :HW]
        dp_ref[:, c0 + 8 * HW:c0 + 9 * HW] = dgo_ref[:, HW:]
        lr = lr_ref[...].astype(BF16)
        xg = _dot(lr, wgk_ref[...], NN) + bgk_ref[...]
        dxg = []
        for d, (raw_ref, dk_ref, dg_ref) in enumerate(((hf_ref, dkf_ref, dgf_ref), (hb_ref, dkb_ref, dgb_ref))):
            lbd = _hg_lb(lb_ref, d)
            s = _sig(raw_ref[...].astype(F32))
            f = lbd + (1.0 - lbd) * s
            df = dg_ref[:, :HW] / f - dk_ref[:, :HW]
            put(2 + d, df * (1.0 - lbd) * s * (1.0 - s))
            dlb_ref[d:d + 1, :] += _colsum(df * (1.0 - s)) * (lbd * (1.0 - lbd))
            dxg.append(dg_ref[:, HW:] * (1.0 / GLA_NORM) * _sig(-xg[:, d * HW:(d + 1) * HW]))
        dxg = jnp.concatenate(dxg, axis=1)
        db_ref[0:1, :] += _colsum(dxg)
        dxg_b = dxg.astype(BF16)
        dw_ref[...] += _dot(lr, dxg_b, TN)
        dlr = _dot(dxg_b, wgk_ref[...], NT)
        dp_ref[:, LR0:LR0 + DH] = (dlr + dgm_ref[:, :DH].astype(F32)).astype(BF16)
        dp_ref[:, LR0 + DH:GATE_GLA0] = dgm_ref[:, DH:D]
        dp_ref[:, GATE_GLA0:GATE_GLA0 + DH] = dgm_ref[:, D:GW] + dgm_ref[:, GW:GW + DH]
        dp_ref[:, GATE_GLA0 + DH:GATE_GLA0 + GW] = dgm_ref[:, GW + DH:]
        dp_ref[:, GATE_GLA0 + GW:] = jnp.zeros((TR, W_IN_COLS - GATE_GLA0 - GW), BF16)

    return pl.pallas_call(
        body, name="gates_bwd", grid=(t // TR,),
        in_specs=[seg(0), seg(2), seg(3), _rowcol(DH, LR0 // DH), _full((2, 2, HW)), _full((DH, D)), _full((1, D)),
                  _row(2 * GW), _row(D)] + [_row(D)] * 8,
        out_specs=[_row(W_IN_COLS), _full((8, HW)), _full((DH, D)), _full((8, D))],
        out_shape=[jax.ShapeDtypeStruct((t, W_IN_COLS), BF16), jax.ShapeDtypeStruct((8, HW), F32),
                   jax.ShapeDtypeStruct((DH, D), F32), jax.ShapeDtypeStruct((8, D), F32)],
        compiler_params=_cp("arbitrary"),
    )(p, p, p, p, hg_lb, wgk, bgk, dgm, dgo, dq_f, dq_b, dv_f, dv_b, dk_f, dk_b, dg_f, dg_b)


def _scan_consts(rev):
    r = lax.broadcasted_iota(jnp.int32, (CHUNK, CHUNK), 0)
    u = lax.broadcasted_iota(jnp.int32, (CHUNK, CHUNK), 1)
    rp = lax.broadcasted_iota(jnp.int32, (CHUNK, 1), 0)
    if rev:
        r, u, rp = CHUNK - 1 - r, CHUNK - 1 - u, CHUNK - 1 - rp
    tri = jnp.where(u <= r, 1.0, 0.0).astype(F32)
    tri_t = jnp.where(r <= u, 1.0, 0.0).astype(F32)
    lv = []
    for b in LEVELS:
        sh = b.bit_length() - 1
        pair = ((r >> sh) == (u >> sh) + 1) & (((u >> sh) & 1) == 0)
        pair_t = ((u >> sh) == (r >> sh) + 1) & (((r >> sh) & 1) == 0)
        tside = ((rp >> sh) & 1) == 1
        lv.append((pair, pair_t, tside))
    bd = LEVELS[-1].bit_length() - 1
    diag = ((r >> bd) == (u >> bd)) & (u <= r)
    diag_t = ((r >> bd) == (u >> bd)) & (r <= u)
    return tri, tri_t, lv, diag, diag_t


def _row_of(pos, rev):
    return CHUNK - 1 - pos if rev else pos


def _chunk_terms(cum, b_scr, consts, rev):
    _, _, lv, _, _ = consts
    terms = []
    for b, (_, _, tside) in zip(LEVELS, lv):
        pieces = []
        for j in range(CHUNK // (2 * b)):
            row = _row_of(2 * b * j + b - 1, rev)
            pieces.append(jnp.broadcast_to(b_scr[row:row + 1, :], (2 * b, DH)))
        if rev:
            pieces = pieces[::-1]
        bnd = pieces[0] if len(pieces) == 1 else jnp.concatenate(pieces, axis=0)
        w = jnp.exp(jnp.minimum(jnp.where(tside, cum - bnd, bnd - cum), 0.0))
        wq = jnp.where(tside, w, 0.0)
        wk = jnp.where(tside, 0.0, w)
        terms.append((wq, wk))
    b = LEVELS[-1]
    pieces = []
    for j in range(CHUNK // b):
        if j == 0:
            pieces.append(jnp.zeros((b, DH), F32))
        else:
            row = _row_of(b * j - 1, rev)
            pieces.append(jnp.broadcast_to(b_scr[row:row + 1, :], (b, DH)))
    if rev:
        pieces = pieces[::-1]
    start = jnp.concatenate(pieces, axis=0)
    wq = jnp.exp(jnp.minimum(cum - start, 0.0))
    wk = jnp.exp(jnp.minimum(start - cum, EXP_CLAMP))
    terms.append((wq, wk))
    return terms


def _run_staged(units):
    live = list(units)
    while live:
        nxt = []
        for u in live:
            try:
                next(u)
                nxt.append(u)
            except StopIteration:
                pass
        live = nxt


SCAN_TB = 256
SCAN_CB = SCAN_TB // CHUNK


def _block_order(i, ntb, rev):
    nctx = CTX // SCAN_TB
    if not rev:
        return i
    return jnp.where(i < nctx, nctx - 1 - i, ntb - 1 - (i - nctx))


def _chunk_in_block(j, rev):
    return SCAN_CB - 1 - j if rev else j


def _scan_fwd(q, k, v, g, rev):
    t = q.shape[0]
    nc = t // CHUNK
    hpb = SCAN_HEADS_FWD

    def body(q_ref, k_ref, v_ref, g_ref, o_ref, st_ref, s_scr, b_scr):
        consts = _scan_consts(rev)
        _, _, lv, diag, _ = consts
        masks = [pair for pair, _, _ in lv] + [diag]

        @pl.when(pl.program_id(1) == 0)
        def _():
            s_scr[...] = jnp.zeros_like(s_scr)

        tri = consts[0]
        state = {hh: s_scr[hh] for hh in range(hpb)}

        def unit(hh, j):
            sl = slice(hh * DH, (hh + 1) * DH)
            c = _chunk_in_block(j, rev)
            rows = slice(c * CHUNK, (c + 1) * CHUNK)
            b_ref = b_scr.at[hh * SCAN_CB + j]
            qc, kc, vc, gc = q_ref[rows, sl], k_ref[rows, sl], v_ref[rows, sl], g_ref[rows, sl]
            cum = _split_dot(tri, gc)
            b_ref[...] = cum
            yield
            terms = _chunk_terms(cum, b_ref, consts, rev)
            ops = [((qc * wq).astype(BF16), (kc * wk).astype(BF16)) for wq, wk in terms]
            tot = _colsum(gc)
            qe = (qc * jnp.exp(cum)).astype(BF16)
            ke = (kc * jnp.exp(tot - cum)).astype(BF16)
            vb = vc.astype(BF16)
            yield
            scs = [_dot(qt, kt, NT) for qt, kt in ops]
            kv = _dot(vb, ke, TN)
            yield
            a = jnp.zeros((CHUNK, CHUNK), F32)
            for sc, m in zip(scs, masks):
                a = a + jnp.where(m, sc, 0.0)
            o_intra = _dot(a.astype(BF16), vb, NN)
            yield
            st = state[hh]
            st_ref[hh, c] = st
            o_ref[rows, sl] = o_intra + _dot(qe, st.astype(BF16), NT)
            state[hh] = st * jnp.exp(tot) + kv
            yield

        _run_staged([unit(hh, j) for hh in range(hpb) for j in range(SCAN_CB)])
        for hh in range(hpb):
            s_scr[hh] = state[hh]

    ntb = t // SCAN_TB
    col = pl.BlockSpec((SCAN_TB, hpb * DH), lambda h, i: (_block_order(i, ntb, rev), h))
    return pl.pallas_call(
        body, name="scan_fwd_" + ("bw" if rev else "fw"), grid=(NH // hpb, ntb),
        in_specs=[col] * 4,
        out_specs=[col, pl.BlockSpec((hpb, SCAN_CB, DH, DH), lambda h, i: (h, _block_order(i, ntb, rev), 0, 0))],
        out_shape=[jax.ShapeDtypeStruct((t, D), F32), jax.ShapeDtypeStruct((NH, nc, DH, DH), F32)],
        scratch_shapes=[pltpu.VMEM((hpb, DH, DH), F32), pltpu.VMEM((hpb * SCAN_CB, CHUNK, DH), F32)],
        compiler_params=_cp("parallel", "arbitrary"),
    )(q, k, v, g)


def _scan_bwd(q, k, v, g, do, states, rev):
    t = q.shape[0]
    nc = t // CHUNK
    hpb = SCAN_HEADS_BWD

    def body(q_ref, k_ref, v_ref, g_ref, do_ref, st_ref, dq_ref, dk_ref, dv_ref, dg_ref, ds_scr, b_scr):
        consts = _scan_consts(rev)
        _, tri_t, lv, diag, diag_t = consts
        masks = [(pair, pair_t) for pair, pair_t, _ in lv] + [(diag, diag_t)]
        @pl.when(pl.program_id(1) == 0)
        def _():
            ds_scr[...] = jnp.zeros_like(ds_scr)

        tri = consts[0]
        dstate = {hh: ds_scr[hh] for hh in range(hpb)}

        def unit(hh, jj):
            sl = slice(hh * DH, (hh + 1) * DH)
            c = _chunk_in_block(SCAN_CB - 1 - jj, rev)
            rows = slice(c * CHUNK, (c + 1) * CHUNK)
            b_ref = b_scr.at[hh * SCAN_CB + jj]
            qc, kc, vc, gc = q_ref[rows, sl], k_ref[rows, sl], v_ref[rows, sl], g_ref[rows, sl]
            dob = do_ref[rows, sl].astype(BF16)
            vb = vc.astype(BF16)
            cum = _split_dot(tri, gc)
            b_ref[...] = cum
            da = _dot(dob, vb, NT)
            da_t = _dot(vb, dob, NT)
            yield
            terms = _chunk_terms(cum, b_ref, consts, rev)
            ops = [((qc * wq).astype(BF16), (kc * wk).astype(BF16)) for wq, wk in terms]
            tot = _colsum(gc)
            e_tot = jnp.exp(tot)
            e_b = jnp.exp(cum)
            e_t = jnp.exp(tot - cum)
            qeb = (qc * e_b).astype(BF16)
            keb = (kc * e_t).astype(BF16)
            dal = [(jnp.where(m, da, 0.0).astype(BF16), jnp.where(m_t, da_t, 0.0).astype(BF16)) for m, m_t in masks]
            yield
            ats = [_dot(ktb, qtb, NT) for qtb, ktb in ops]
            dqts = [_dot(d, ktb, NN) for (d, _), (_, ktb) in zip(dal, ops)]
            dkts = [_dot(d_t, qtb, NN) for (_, d_t), (qtb, _) in zip(dal, ops)]
            qd = _dot(dob, qeb, TN)
            yield
            a_t = jnp.zeros((CHUNK, CHUNK), F32)
            dq = jnp.zeros((CHUNK, DH), F32)
            dk = jnp.zeros((CHUNK, DH), F32)
            db = jnp.zeros((CHUNK, DH), F32)
            for at, dqt, dkt, (wq, wk), (qtb, ktb), (_, m_t) in zip(ats, dqts, dkts, terms, ops, masks):
                a_t = a_t + jnp.where(m_t, at, 0.0)
                dq = dq + dqt * wq
                dk = dk + dkt * wk
                db = db + dqt * qtb.astype(F32) - dkt * ktb.astype(F32)
            dv_intra = _dot(a_t.astype(BF16), dob, NN)
            st = st_ref[hh, c]
            stb = st.astype(BF16)
            dqe = _dot(dob, stb, NN)
            yield
            dst = dstate[hh]
            dstb = dst.astype(BF16)
            dstate[hh] = dst * e_tot + qd
            dv_ref[rows, sl] = dv_intra + _dot(keb, dstb, NT)
            dke = _dot(vb, dstb, NN)
            yield
            qe = qeb.astype(F32)
            ke = keb.astype(F32)
            dq_ref[rows, sl] = dq + dqe * e_b
            dk_ref[rows, sl] = dk + dke * e_t
            db = db + dqe * qe - dke * ke
            dtot = _colsum(dstb.astype(F32) * stb.astype(F32)) * e_tot + _colsum(dke * ke)
            dg_ref[rows, sl] = _split_dot(tri_t, db) + dtot
            yield

        _run_staged([unit(hh, jj) for hh in range(hpb) for jj in range(SCAN_CB)])
        for hh in range(hpb):
            ds_scr[hh] = dstate[hh]

    ntb = t // SCAN_TB
    blk = lambda i: _block_order(ntb - 1 - i, ntb, rev)
    col = pl.BlockSpec((SCAN_TB, hpb * DH), lambda h, i: (blk(i), h))
    out = jax.ShapeDtypeStruct((t, D), F32)
    return pl.pallas_call(
        body, name="scan_bwd_" + ("bw" if rev else "fw"), grid=(NH // hpb, ntb),
        in_specs=[col] * 5 + [pl.BlockSpec((hpb, SCAN_CB, DH, DH), lambda h, i: (h, blk(i), 0, 0))],
        out_specs=[col] * 4,
        out_shape=[out] * 4,
        scratch_shapes=[pltpu.VMEM((hpb, DH, DH), F32), pltpu.VMEM((hpb * SCAN_CB, CHUNK, DH), F32)],
        compiler_params=_cp("parallel", "arbitrary"),
    )(q, k, v, g, do, states)


W_IN_GRAD_ROWS = ((0, 640), (640, 736), (736, 1024))
W_IN_REF = 6688


def _layout_w_in(w):
    return jnp.pad(w, ((0, 0), (0, W_IN_COLS - W_IN_REF)))


def _unlayout_w_in(d):
    return d[:, :W_IN_REF]


def _gate_cols(w):
    return jnp.pad(w, ((0, 0), (GOFF, GW - GOFF - D)))


def _gate_rows(w):
    return jnp.pad(w, ((GOFF, GW - GOFF - D), (0, 0)))


def _layout_wgk(w):
    r = w.shape[1]
    top = jnp.concatenate([w[0], jnp.zeros_like(w[0])], axis=1)
    bot = jnp.concatenate([jnp.zeros_like(w[1]), w[1]], axis=1)
    return jnp.concatenate([top, bot, jnp.zeros((DH - 2 * r, D), w.dtype)], axis=0)


def _unlayout_wgk(d, r=16):
    return jnp.stack([d[:r, :HW], d[r:2 * r, HW:]])


def _local_step(z, target, modc, modx, norms, onw, hg_lb, wgk, bgk, w_in, get_mix, get_ffn, send):
    n_pre1, n_post1, n_pre2, n_post2 = norms
    t = z.shape[0]
    tm = 768 if t % 768 == 0 else 256
    h1 = _prenorm(z, n_pre1, modc, modx, 0, 1, "prenorm1")
    p = _matmul(h1, w_in, NN, BF16, "mm_in", tm, 512, D)
    q, v, k_f, k_b, g_f, g_b = _gates_fwd(p, hg_lb, wgk, bgk)
    o_f, st_f = _scan_fwd(q, k_f, v, g_f, False)
    o_b, st_b = _scan_fwd(q, k_b, v, g_b, True)
    y = _post_fwd(o_f, o_b, p, onw)
    w_br_hg, w_br_gla, w_out = get_mix(y)
    u1 = _matmul(y, w_br_hg, NN, BF16, "mm_br_hg", tm, GW, HW, a_off=0)
    u2 = _matmul(y, w_br_gla, NN, BF16, "mm_br_gla", tm, GW, HW, a_off=1)
    merged = _merge_fwd(p, u1, u2)
    y1 = _matmul(merged, w_out, NN, F32, "mm_out", tm, 512, GW)
    z1, h2 = _mid_fwd(z, y1, n_post1, n_pre2, modc, modx)
    w_gu_t, w_down = get_ffn(h2)
    uv = _matmul(h2, w_gu_t, NT, BF16, "mm_gu", tm, 512, D)
    act = _swiglu_fwd(uv)
    y2 = _matmul(act, w_down, NN, F32, "mm_down", tm, 512, D_FF // 2)
    dz, dy2, loss_vec, sm_final = _final(z1, y2, target, n_post2, modc, modx)
    dact = _matmul(dy2, w_down, NT, BF16, "mm_down_dx", tm, D_FF // 2, D)
    d_w_down = _matmul(act, dy2, TN, BF16, "mm_down_dw", D_FF // 2, 512, t)
    duv = _swiglu_bwd(uv, dact)
    dh2 = _matmul(duv, w_gu_t, NN, F32, "mm_gu_dx", tm, 512, D_FF // 2)
    d_w_gu_t = _matmul(duv, h2, TN, BF16, "mm_gu_dw", 512, 512, t)
    dh2 = send(("w_down", "w_gu_t"), (d_w_down, d_w_gu_t), dh2)
    dz, dy1, sm_mid = _mid_bwd(dh2, dz, z, z1, y1, n_post1, n_pre2, modc, modx)
    dmerged = _matmul(dy1, w_out, NT, F32, "mm_out_dx", tm, GW, D)
    d_w_out = _matmul(merged, dy1, TN, BF16, "mm_out_dw", GW, 512, t)
    du1, du2, dgm = _merge_bwd(dmerged, p, u1, u2)
    dy_hg = _matmul(du1, w_br_hg, NT, F32, "mm_br_hg_dx", tm, HW, GW)
    dy_gla = _matmul(du2, w_br_gla, NT, F32, "mm_br_gla_dx", tm, HW, GW)
    d_w_br_hg = _matmul(y, du1, TN, BF16, "mm_br_hg_dw", HW, GW, t, a_off=0, m_out=HW)
    d_w_br_gla = _matmul(y, du2, TN, BF16, "mm_br_gla_dw", HW, GW, t, a_off=1, m_out=HW)
    dy_hg = send(("w_out", "w_br_hg", "w_br_gla"), (d_w_out, d_w_br_hg, d_w_br_gla), dy_hg)
    do, dgo, sm_post = _post_bwd(dy_hg, dy_gla, o_f, o_b, p, onw)
    dq_f, dk_f, dv_f, dg_f = _scan_bwd(q, k_f, v, g_f, do, st_f, False)
    dq_b, dk_b, dv_b, dg_b = _scan_bwd(q, k_b, v, g_b, do, st_b, True)
    dp, d_lb, d_wgk, d_bgk = _gates_bwd(p, hg_lb, wgk, bgk, dgm, dgo, dq_f, dq_b, dv_f, dv_b, dk_f, dk_b, dg_f, dg_b)
    d_w_in = _matmul(h1, dp, TN, BF16, "mm_in_dw", 512, 512, t)
    dp = send(("w_in",), (d_w_in,), dp)
    dh1 = _matmul(dp, w_in, NT, F32, "mm_in_dx", tm, 512, 1024)
    grad_x, sm_pre = _pre_bwd(dh1, dz, z, n_pre1, modc, modx)
    return dict(loss_vec=loss_vec, grad_x=grad_x, sm_final=sm_final, sm_mid=sm_mid, sm_post=sm_post, sm_pre=sm_pre,
                d_lb=d_lb, d_wgk=d_wgk, d_bgk=d_bgk)


MESH = pl.DeviceIdType.MESH
ANY = pl.BlockSpec(memory_space=pl.ANY)
N_REL = N_DEV - 1


def _place():
    return lax.axis_index("x"), lax.axis_index("y"), lax.axis_index("c")


def _slot(p):
    return 4 * p[0] + 2 * p[1] + p[2]


def _all_gather(arrays, name):
    n = len(arrays)

    def body(*refs):
        ins, outs = refs[:n], refs[n:2 * n]
        send_sems, recv_sems, local_sems = refs[2 * n:]
        x, y, c = _place()
        me, sibling = (x, y, c), (x, y, 1 - c)
        chips = [(1 - x, y), (x, 1 - y), (1 - x, 1 - y)]

        def copy(a, k, block, to, src=None):
            dst = outs[a].at[_slot(block)]
            return pltpu.make_async_remote_copy(
                src_ref=dst if src is None else src, dst_ref=dst,
                send_sem=send_sems.at[N_REL * a + k], recv_sem=recv_sems.at[N_REL * a + k],
                device_id=to, device_id_type=MESH)

        mine = [pltpu.make_async_copy(ins[a], outs[a].at[_slot(me)], local_sems.at[a]) for a in range(n)]
        for cp in mine:
            cp.start()
        first = []
        for a in range(n):
            first.append(copy(a, 0, me, sibling, src=ins[a]))
            first += [copy(a, 1 + j, me, (*chip, c), src=ins[a]) for j, chip in enumerate(chips)]
        for cp in first:
            cp.start()
        passed = []
        for j, chip in enumerate(chips):
            for a in range(n):
                copy(a, 1 + j, (*chip, c), me).wait_recv()
                fwd = copy(a, 4 + j, (*chip, c), sibling)
                fwd.start()
                passed.append(fwd)
        for a in range(n):
            copy(a, 0, sibling, me).wait_recv()
        for j, chip in enumerate(chips):
            for a in range(n):
                copy(a, 4 + j, (*chip, 1 - c), me).wait_recv()
        for cp in first + passed:
            cp.wait_send()
        for cp in mine:
            cp.wait()

    return pl.pallas_call(
        body, name=name,
        in_specs=[ANY] * n, out_specs=[ANY] * n,
        out_shape=[jax.ShapeDtypeStruct((N_DEV,) + a.shape, a.dtype) for a in arrays],
        scratch_shapes=[pltpu.SemaphoreType.DMA((N_REL * n,)), pltpu.SemaphoreType.DMA((N_REL * n,)),
                        pltpu.SemaphoreType.DMA((n,))],
    )(*arrays)


def _exchange(arrays, name):
    n = len(arrays)

    def body(*refs):
        ins, outs = refs[:n], refs[n:2 * n]
        send_sems, recv_sems, local_sems = refs[2 * n:]
        x, y, c = _place()
        me = _slot((x, y, c))
        mine = [pltpu.make_async_copy(ins[a].at[me], outs[a].at[me], local_sems.at[a]) for a in range(n)]
        for cp in mine:
            cp.start()
        copies = []
        for a in range(n):
            for k in range(1, N_DEV):
                flip = lambda v, bit: 1 - v if bit else v
                peer = (flip(x, k & 4), flip(y, k & 2), flip(c, k & 1))
                copies.append(pltpu.make_async_remote_copy(
                    src_ref=ins[a].at[_slot(peer)], dst_ref=outs[a].at[me],
                    send_sem=send_sems.at[N_REL * a + k - 1], recv_sem=recv_sems.at[N_REL * a + k - 1],
                    device_id=peer, device_id_type=MESH))
                copies[-1].start()
        i = 0
        for a in range(n):
            for k in range(1, N_DEV):
                flip = lambda v, bit: 1 - v if bit else v
                peer = (flip(x, k & 4), flip(y, k & 2), flip(c, k & 1))
                pltpu.make_async_remote_copy(
                    src_ref=ins[a].at[_slot(peer)], dst_ref=outs[a].at[_slot(peer)],
                    send_sem=send_sems.at[N_REL * a + k - 1], recv_sem=recv_sems.at[N_REL * a + k - 1],
                    device_id=peer, device_id_type=MESH).wait_recv()
                i += 1
        for cp in copies:
            cp.wait_send()
        for cp in mine:
            cp.wait()

    return pl.pallas_call(
        body, name=name,
        in_specs=[ANY] * n, out_specs=[ANY] * n,
        out_shape=[jax.ShapeDtypeStruct(a.shape, a.dtype) for a in arrays],
        scratch_shapes=[pltpu.SemaphoreType.DMA((N_REL * n,)), pltpu.SemaphoreType.DMA((N_REL * n,)),
                        pltpu.SemaphoreType.DMA((n,))],
    )(*arrays)


HBM = pl.BlockSpec(memory_space=pltpu.HBM)
SEM = pl.BlockSpec(memory_space=pltpu.SEMAPHORE)
EFFECT = pltpu.SideEffectType.DATAFLOW_SIDE_EFFECTING


def _peer_of(x, y, c, k):
    flip = lambda v, bit: 1 - v if bit else v
    return flip(x, k & 4), flip(y, k & 2), flip(c, k & 1)


def _view_whole(src, slot):
    return src


def _view_block(src, slot):
    return src.at[slot]


W_IN_SHARD = W_IN_REF // N_DEV


def _view_window(rows):
    def view(src, slot):
        col0 = pl.multiple_of((W_IN_SHARD * slot // DH) * DH, DH)
        return src.at[pl.ds(rows[0], rows[1] - rows[0]), pl.ds(col0, D)]
    return view


def _split_copies(view, srcs, lands, send_sems, recv_sems, local_sems):
    x, y, c = _place()
    me = _slot((x, y, c))
    local, sends, waits = [], [], []
    for a, (src, land) in enumerate(zip(srcs, lands)):
        local.append(pltpu.make_async_copy(view(src, me), land.at[me], local_sems.at[a]))
        for k in range(1, N_DEV):
            peer = _peer_of(x, y, c, k)
            mine = view(src, _slot(peer))
            sems = dict(send_sem=send_sems.at[N_REL * a + k - 1], recv_sem=recv_sems.at[N_REL * a + k - 1],
                        device_id=peer, device_id_type=MESH)
            sends.append(pltpu.make_async_remote_copy(src_ref=mine, dst_ref=land.at[me], **sems))
            waits.append(pltpu.make_async_remote_copy(src_ref=mine, dst_ref=land.at[_slot(peer)], **sems))
    return local, sends, waits


def _split_start(view, land_shapes, srcs, name, after):
    n = len(srcs)
    lands = [lax.empty(shp, s.dtype) for shp, s in zip(land_shapes, srcs)]

    def body(*refs):
        src_refs, land_refs = refs[:n], refs[n:2 * n]
        send_sems, recv_sems, local_sems = refs[2 * n + 1:2 * n + 4]
        token = refs[-1]
        local, sends, _ = _split_copies(view, src_refs, land_refs, send_sems, recv_sems, local_sems)
        for cp in local + sends:
            cp.start()
        token[...] = jnp.zeros_like(token)

    hbm = lambda a: pltpu.with_memory_space_constraint(a, pltpu.HBM)
    out = pl.pallas_call(
        body, name=name,
        out_shape=(pltpu.SemaphoreType.DMA((N_REL * n,)), pltpu.SemaphoreType.DMA((N_REL * n,)),
                   pltpu.SemaphoreType.DMA((n,)),
                   *[pltpu.HBM(s.shape, s.dtype) for s in srcs], *[pltpu.HBM(l.shape, l.dtype) for l in lands],
                   jax.ShapeDtypeStruct((8, DH), F32)),
        in_specs=[HBM] * (2 * n) + [ANY],
        out_specs=(SEM, SEM, SEM, *([HBM] * (2 * n)), pl.BlockSpec(memory_space=pltpu.VMEM)),
        input_output_aliases={i: 3 + i for i in range(2 * n)},
        compiler_params=pltpu.CompilerParams(has_side_effects=EFFECT),
    )(*[hbm(s) for s in srcs], *[hbm(l) for l in lands], after)
    handle = dict(view=view, n=n, sems=out[:3], srcs=list(out[3:3 + n]), lands=list(out[3 + n:3 + 2 * n]))
    return handle, out[-1]


def _split_wait(handle, name, after, srcs=None):
    view, n, sems, lands = handle["view"], handle["n"], handle["sems"], handle["lands"]
    srcs = handle["srcs"] if srcs is None else srcs

    def body(*refs):
        src_refs, land_refs = refs[:n], refs[n:2 * n]
        send_sems, recv_sems, local_sems = refs[2 * n:2 * n + 3]
        local, _, waits = _split_copies(view, src_refs, land_refs, send_sems, recv_sems, local_sems)
        for cp in waits:
            cp.wait_send()
            cp.wait_recv()
        for cp in local:
            cp.wait()

    out = pl.pallas_call(
        body, name=name,
        out_shape=(*[pltpu.HBM(s.shape, s.dtype) for s in srcs], *[pltpu.HBM(l.shape, l.dtype) for l in lands]),
        in_specs=[HBM] * (2 * n) + [SEM, SEM, SEM, ANY],
        out_specs=tuple([HBM] * (2 * n)),
        input_output_aliases={i: i for i in range(2 * n)},
        compiler_params=pltpu.CompilerParams(has_side_effects=EFFECT),
    )(*srcs, *lands, *sems, after)
    handle["srcs"] = list(out[:n])
    return list(out[n:])


def _tie(x, token, name):
    def body(x_ref, t_ref, o_ref):
        pass

    return pl.pallas_call(
        body, name=name, out_shape=jax.ShapeDtypeStruct(x.shape, x.dtype),
        in_specs=[ANY, ANY], out_specs=ANY, input_output_aliases={0: 0},
    )(x, token)


def _mod_fwd(a, w, b):
    def body(a_ref, w_ref, b_ref, o_ref):
        o_ref[...] = _dot(_silu(a_ref[...]), w_ref[...], NN, precision=HI) + b_ref[...]

    return pl.pallas_call(
        body, name="mod_fwd", out_shape=jax.ShapeDtypeStruct((a.shape[0], w.shape[1]), F32),
        compiler_params=pltpu.CompilerParams(vmem_limit_bytes=VMEM_LIMIT),
    )(a, w, b)


def _mod_bwd(a, d, w):
    def body(a_ref, d_ref, w_ref, dw_ref, dc_ref):
        av = a_ref[...]
        dv = d_ref[...]
        dw_ref[...] = _dot(_silu(av), dv, TN, precision=HI)
        da = _dot(dv[0:8, :], w_ref[...], NT, precision=HI) * _dsilu(av[0:8, :])
        row = lax.broadcasted_iota(jnp.int32, da.shape, 0)
        dc_ref[...] = jnp.where(row == 0, da, 0.0)

    return pl.pallas_call(
        body, name="mod_bwd",
        out_shape=[jax.ShapeDtypeStruct(w.shape, F32), jax.ShapeDtypeStruct((8, w.shape[0]), F32)],
        compiler_params=pltpu.CompilerParams(vmem_limit_bytes=VMEM_LIMIT),
    )(a, d, w)


def _sum_devices(g):
    def body(g_ref, o_ref):
        acc = g_ref[0]
        for i in range(1, g.shape[0]):
            acc = acc + g_ref[i]
        o_ref[...] = acc

    return pl.pallas_call(body, name="sum_devices_%d" % g.shape[1],
                          out_shape=jax.ShapeDtypeStruct(g.shape[1:], F32))(g)


def _sum_windows(g):
    n, r, c = g.shape
    tr = 128

    def body(g_ref, o_ref):
        x, y, cc = _place()
        lane0 = (W_IN_SHARD * _slot((x, y, cc))) % DH
        acc = g_ref[0].astype(F32)
        for i in range(1, n):
            acc = acc + g_ref[i].astype(F32)
        o_ref[...] = pltpu.roll(acc, (c - lane0) % c, 1)

    return pl.pallas_call(
        body, name="sum_windows", grid=(r // tr,),
        in_specs=[pl.BlockSpec((n, tr, c), lambda i: (0, i, 0))],
        out_specs=pl.BlockSpec((tr, c), lambda i: (i, 0)),
        out_shape=jax.ShapeDtypeStruct((r, c), F32),
        compiler_params=_cp("parallel"),
    )(g)


def _adam_rows(r, c, n):
    budget = 6 * 1024 * 1024
    best = None
    for tr in range(16, r + 1, 16):
        if r % tr == 0 and tr * c * (2 * n + 28) <= budget:
            best = tr
    return best if best is not None else r


def _adamw(g, w, m, v, name):
    n, r, c = g.shape
    tr = _adam_rows(r, c, n)
    bc1 = 1.0 - ADAM_B1 ** ADAM_STEP
    bc2 = 1.0 - ADAM_B2 ** ADAM_STEP

    def body(g_ref, w_ref, m_ref, v_ref, go_ref, d_ref, mo_ref, vo_ref):
        grad = g_ref[0].astype(F32)
        for i in range(1, n):
            grad = grad + g_ref[i].astype(F32)
        go_ref[...] = grad
        m_new = ADAM_B1 * m_ref[...] + (1.0 - ADAM_B1) * grad
        v_new = ADAM_B2 * v_ref[...] + (1.0 - ADAM_B2) * (grad * grad)
        mo_ref[...] = m_new
        vo_ref[...] = v_new
        d_ref[...] = -ADAM_LR * ((m_new / bc1) / (jnp.sqrt(v_new / bc2) + ADAM_EPS) + ADAM_WD * w_ref[...])

    blk = pl.BlockSpec((tr, c), lambda i: (i, 0))
    out = jax.ShapeDtypeStruct((r, c), F32)
    return pl.pallas_call(
        body, name=name, grid=(r // tr,),
        in_specs=[pl.BlockSpec((n, tr, c), lambda i: (0, i, 0)), blk, blk, blk],
        out_specs=[blk] * 4, out_shape=[out] * 4,
        compiler_params=_cp("parallel"),
    )(g, w, m, v)


def kernel(x, c, ctx, c_ctx, w_mod, b_mod, norm_pre1, norm_post1, norm_pre2, norm_post2, w_in, hg_lb, hg_onorm, gla_w_gk, gla_b_gk, gla_onorm, w_br_hg, w_br_gla, w_out, w_ff_gate, w_ff_up, w_ff_down, loss_target, m_c_ctx, m_w_mod, m_b_mod, m_norm_pre1, m_norm_post1, m_norm_pre2, m_norm_post2, m_w_in, m_hg_lb, m_hg_onorm, m_gla_w_gk, m_gla_b_gk, m_gla_onorm, m_w_br_hg, m_w_br_gla, m_w_out, m_w_ff_gate, m_w_ff_up, m_w_ff_down, v_c_ctx, v_w_mod, v_b_mod, v_norm_pre1, v_norm_post1, v_norm_pre2, v_norm_post2, v_w_in, v_hg_lb, v_hg_onorm, v_gla_w_gk, v_gla_b_gk, v_gla_onorm, v_w_br_hg, v_w_br_gla, v_w_out, v_w_ff_gate, v_w_ff_up, v_w_ff_down):
    xi, yi, ci = lax.axis_index("x"), lax.axis_index("y"), lax.axis_index("c")
    me = 4 * xi + 2 * yi + ci
    t = CTX + x.shape[1]

    c_all, lb_g, wgk_g, bgk_g = _all_gather([c, hg_lb, gla_w_gk[0], gla_b_gk[0]], "ag_small")
    tr_ = lambda a: jnp.swapaxes(a[0], 0, 1)
    big = [w_in[0], w_br_hg[0], w_br_gla[0], w_out[0], tr_(w_ff_gate), tr_(w_ff_up), w_ff_down[0]]
    big_bf = [w.astype(BF16) for w in big]
    g_in, = _all_gather(big_bf[:1], "ag_w_in")
    gathered = lambda arrs: [(N_DEV,) + a.shape for a in arrs]
    mix_handle, tok = _split_start(_view_whole, gathered(big_bf[1:4]), big_bf[1:4], "ag_mix_start", g_in)
    ffn_handle, tok = _split_start(_view_whole, gathered(big_bf[4:]), big_bf[4:], "ag_ffn_start", tok)
    cols = lambda g: jnp.transpose(g, (1, 0, 2)).reshape(g.shape[1], N_DEV * g.shape[2])
    w_in_k = _tie(_layout_w_in(cols(g_in)), tok, "tie_w_in")

    def get_mix(after):
        g_brh, g_brg, g_out = _split_wait(mix_handle, "ag_mix_wait", after)
        return _gate_cols(cols(g_brh)), _gate_cols(cols(g_brg)), _gate_rows(g_out.reshape(D, D))

    def get_ffn(after):
        g_gate, g_up, g_down = _split_wait(ffn_handle, "ag_ffn_wait", after)
        return jnp.concatenate([g_gate.reshape(D_FF, D), g_up.reshape(D_FF, D)], axis=0), g_down.reshape(D_FF, D)

    hg_lb_full = jnp.transpose(lb_g, (1, 2, 0, 3)).reshape(2, 2, HW)
    wgk_k = _layout_wgk(jnp.transpose(wgk_g, (1, 2, 0, 3)).reshape(2, 16, HW)).astype(BF16)
    bgk_k = jnp.transpose(bgk_g, (1, 0, 2)).reshape(1, D)
    onw = jnp.concatenate([jnp.tile(hg_onorm, (1, NH // 2)), jnp.tile(gla_onorm, (1, NH // 2))], axis=1)

    n_mod = w_mod.shape[2]
    a9 = jnp.concatenate([c_ctx[None], c_all[:, 0], jnp.zeros((16 - 1 - N_DEV, D), F32)], axis=0)
    b_loc = lax.dynamic_slice(b_mod, (0, me * n_mod), (1, n_mod))
    s_loc = _mod_fwd(a9, w_mod[0], b_loc)
    s_all, = _all_gather([s_loc], "ag_mod")
    mod_all = jnp.transpose(s_all, (1, 0, 2)).reshape(16, N_DEV * n_mod)
    pad8 = lambda m: jnp.concatenate([m.reshape(6, D), jnp.zeros((2, D), F32)], axis=0)
    modc = pad8(mod_all[0])
    modx = pad8(lax.dynamic_slice(mod_all, (1 + me, 0), (1, N_DEV * n_mod))[0])

    z = jnp.concatenate([ctx[0], x[0]], axis=0)
    norms = (norm_pre1, norm_post1, norm_pre2, norm_post2)
    shard = lambda d: jnp.transpose(d.reshape(d.shape[0], N_DEV, -1), (1, 0, 2)).astype(BF16)
    rowshard = lambda d: d.reshape(N_DEV, d.shape[0] // N_DEV, d.shape[1]).astype(BF16)
    sent, w_in_grad = [], []

    def send_w_in(i, x_after):
        rows = W_IN_GRAD_ROWS[i]
        handle, tok = _split_start(_view_window(rows), [(N_DEV, rows[1] - rows[0], D)], w_in_grad,
                                   "grads_w_in%d_start" % i, x_after)
        w_in_grad[:] = handle["srcs"]
        sent.append(("w_in%d" % i, ["w_in#%d" % i], handle))
        return _tie(x_after, tok, "tie_w_in%d" % i)

    def send(names, grads, x_after):
        if names == ("w_in",):
            w_in_grad[:] = list(grads)
            return send_w_in(0, x_after)
        arrs, leaves = [], []
        for nm, g in zip(names, grads):
            if nm == "w_gu_t":
                arrs += [rowshard(g[:D_FF]), rowshard(g[D_FF:])]
                leaves += ["w_ff_gate", "w_ff_up"]
            elif nm == "w_down":
                arrs.append(rowshard(g))
                leaves.append("w_ff_down")
            elif nm == "w_out":
                arrs.append(rowshard(g[GOFF:GOFF + D]))
                leaves.append(nm)
            else:
                arrs.append(shard(g[:, GOFF:GOFF + D]))
                leaves.append(nm)
        handle, tok = _split_start(_view_block, [a.shape for a in arrs], arrs, "grads_%s_start" % names[0], x_after)
        sent.append((names[0], leaves, handle))
        return _tie(x_after, tok, "tie_" + names[0])

    r = _local_step(z, loss_target[0], modc, modx, norms, onw, hg_lb_full, wgk_k, bgk_k,
                    w_in_k, get_mix, get_ffn, send)
    grad_x = r["grad_x"][None]

    sm_pre, sm_mid, sm_fin = r["sm_pre"], r["sm_mid"], r["sm_final"]
    dmodc = jnp.stack([sm_pre[0], sm_pre[2], sm_mid[4], sm_mid[0], sm_mid[2], sm_fin[0]]).reshape(-1)
    dmodx = jnp.stack([sm_pre[1], sm_pre[3], sm_mid[5], sm_mid[1], sm_mid[3], sm_fin[1]]).reshape(-1)
    on = r["sm_post"][0].reshape(NH, DH)
    pieces = [dmodc, dmodx, sm_pre[4], sm_mid[7], sm_mid[6], sm_fin[2], on[:NH // 2].sum(0), on[NH // 2:].sum(0),
              r["d_lb"][:2].reshape(-1), _unlayout_wgk(r["d_wgk"]).reshape(-1), r["d_bgk"][0]]
    loss_local = (0.5 / D) * jnp.sum(r["loss_vec"])
    pieces.append(jnp.concatenate([loss_local.reshape(1), jnp.zeros((DH - 1,), F32)]))
    sizes = [p.shape[0] for p in pieces]
    pack = jnp.concatenate(pieces).reshape(-1, DH)
    pack_all, = _all_gather([pack], "ag_small_grads")
    pack_all = send_w_in(1, pack_all)
    tot = _sum_devices(pack_all).reshape(-1)
    offs = [sum(sizes[:i]) for i in range(len(sizes))]
    part = lambda i: tot[offs[i]:offs[i] + sizes[i]]
    dmodc_t, dmodx_t = part(0), part(1)
    g_b_mod = (dmodc_t + dmodx_t)[None]
    g_norms = [part(i)[None] for i in (2, 3, 4, 5)]
    g_hg_on, g_gla_on = part(6)[None], part(7)[None]
    lb0 = lax.dynamic_slice(part(8).reshape(2, HW), (0, me * (HW // N_DEV)), (2, HW // N_DEV))
    g_hg_lb = jnp.stack([lb0, -lb0])
    g_wgk = lax.dynamic_slice(part(9).reshape(2, 16, HW), (0, 0, me * (HW // N_DEV)), (2, 16, HW // N_DEV))[None]
    g_bgk = lax.dynamic_slice(part(10).reshape(2, HW), (0, me * (HW // N_DEV)), (2, HW // N_DEV))[None]
    loss = part(11)[0]

    dmx_all = pack_all.reshape(N_DEV, -1)[:, sizes[0]:sizes[0] + sizes[1]]
    d9 = jnp.concatenate([lax.dynamic_slice(dmodc_t[None], (0, me * n_mod), (1, n_mod)),
                          lax.dynamic_slice(dmx_all, (0, me * n_mod), (N_DEV, n_mod)),
                          jnp.zeros((16 - 1 - N_DEV, n_mod), F32)], axis=0)
    g_w_mod, dcc_part = _mod_bwd(a9, d9, w_mod[0])
    dcc_all, = _all_gather([dcc_part], "ag_c_ctx")
    dcc_all = send_w_in(2, dcc_all)
    g_c_ctx = _sum_devices(dcc_all)[0]

    recv = {}
    for first, leaves, handle in sent:
        if not first.startswith("w_in"):
            recv.update(zip(leaves, _split_wait(handle, "grads_%s_wait" % first, g_c_ctx)))
    moms = [(m_w_in, v_w_in), (m_w_br_hg, v_w_br_hg), (m_w_br_gla, v_w_br_gla), (m_w_out, v_w_out),
            (m_w_ff_gate, v_w_ff_gate), (m_w_ff_up, v_w_ff_up), (m_w_ff_down, v_w_ff_down)]
    names = ["w_in", "w_br_hg", "w_br_gla", "w_out", "w_ff_gate", "w_ff_up", "w_ff_down"]
    res = {}

    def update(nm, w, m, v):
        if nm in ("w_ff_gate", "w_ff_up"):
            outs = _adamw(recv[nm], w, tr_(m), tr_(v), "adamw_" + nm)
            res[nm] = [jnp.swapaxes(o, 0, 1)[None] for o in outs]
        else:
            res[nm] = [o[None] for o in _adamw(recv[nm], w, m[0], v[0], "adamw_" + nm)]

    for nm, w, (m, v) in list(zip(names, big, moms))[1:]:
        update(nm, w, m, v)
    res["w_mod"] = [o[None] for o in _adamw(g_w_mod[None], w_mod[0], m_w_mod[0], v_w_mod[0], "adamw_w_mod")]

    small = [("c_ctx", c_ctx, m_c_ctx, v_c_ctx, g_c_ctx), ("b_mod", b_mod, m_b_mod, v_b_mod, g_b_mod),
             ("norm_pre1", norm_pre1, m_norm_pre1, v_norm_pre1, g_norms[0]),
             ("norm_post1", norm_post1, m_norm_post1, v_norm_post1, g_norms[1]),
             ("norm_pre2", norm_pre2, m_norm_pre2, v_norm_pre2, g_norms[2]),
             ("norm_post2", norm_post2, m_norm_post2, v_norm_post2, g_norms[3]),
             ("hg_lb", hg_lb, m_hg_lb, v_hg_lb, g_hg_lb), ("hg_onorm", hg_onorm, m_hg_onorm, v_hg_onorm, g_hg_on),
             ("gla_w_gk", gla_w_gk, m_gla_w_gk, v_gla_w_gk, g_wgk), ("gla_b_gk", gla_b_gk, m_gla_b_gk, v_gla_b_gk, g_bgk),
             ("gla_onorm", gla_onorm, m_gla_onorm, v_gla_onorm, g_gla_on)]
    flat = lambda k: jnp.concatenate([s[k].reshape(-1) for s in small]).reshape(-1, DH)
    outs = _adamw(flat(4)[None], flat(1), flat(2), flat(3), "adamw_small")
    off = 0
    for nm, w, _, _, _ in small:
        res[nm] = [o.reshape(-1)[off:off + w.size].reshape(w.shape) for o in outs]
        off += w.size

    for first, leaves, handle in sent:
        if first.startswith("w_in"):
            recv.update(zip(leaves, _split_wait(handle, "grads_%s_wait" % first, outs[0], srcs=w_in_grad)))
            w_in_grad[:] = handle["srcs"]
    windows = jnp.concatenate([recv.pop("w_in#%d" % i) for i in range(len(W_IN_GRAD_ROWS))], axis=1)
    recv["w_in"] = _sum_windows(windows)[None, :, :W_IN_SHARD]
    update("w_in", big[0], *moms[0])

    order = ["c_ctx", "w_mod", "b_mod", "norm_pre1", "norm_post1", "norm_pre2", "norm_post2", "w_in", "hg_lb",
             "hg_onorm", "gla_w_gk", "gla_b_gk", "gla_onorm", "w_br_hg", "w_br_gla", "w_out", "w_ff_gate", "w_ff_up",
             "w_ff_down"]
    return (loss, grad_x, *[res[n][k] for k in range(4) for n in order])
```

```python
import functools

import jax
import jax.numpy as jnp
from jax import lax
from jax.experimental import pallas as pl
from jax.experimental.pallas import tpu as pltpu

F32 = jnp.float32
BF16 = jnp.bfloat16
HI = lax.Precision.HIGHEST

N_DEV = 8
D = 1024
CTX = 256
HW = 512
DH = 128
NH = 8
D_FF = 2816
EPS = 1e-6
GLA_NORM = 16.0
CHUNK = 64
TR = 256
NCT = CTX // TR
W_IN_COLS = 7168
MAIN0 = 0
LR0 = 4608
GW = 1152
GOFF = 32
GATE_HG0 = LR0
GATE_GLA0 = LR0 + D
LEVELS = (32, 16, 8)
EXP_CLAMP = 80.0
VMEM_LIMIT = 48 * 1024 * 1024

ADAM_LR, ADAM_B1, ADAM_B2, ADAM_EPS, ADAM_WD, ADAM_STEP = 0.001, 0.9, 0.999, 1e-08, 0.01, 10


def _cp(*sem):
    return pltpu.CompilerParams(dimension_semantics=sem, vmem_limit_bytes=VMEM_LIMIT)


def _sig(x):
    return jax.nn.sigmoid(x)


def _silu(x):
    return x * _sig(x)


def _dsilu(x):
    s = _sig(x)
    return s * (1.0 + x * (1.0 - s))


def _rstd(x):
    return lax.rsqrt(jnp.mean(x * x, axis=-1, keepdims=True) + EPS)


def _rms_bwd(a, y, r):
    return r * (a - y * (r * r) * jnp.mean(a * y, axis=-1, keepdims=True))


def _colsum(x):
    return jnp.sum(x, axis=0, keepdims=True)


def _dot(a, b, dims, precision=None):
    return lax.dot_general(a, b, (dims, ((), ())), preferred_element_type=F32, precision=precision)


NN = ((1,), (0,))
NT = ((1,), (1,))
TN = ((0,), (0,))

SCAN_HEADS_FWD = 4
SCAN_HEADS_BWD = 4


def _split_dot(m, x):
    mb = m.astype(BF16)
    x1 = x.astype(BF16)
    r1 = x - x1.astype(F32)
    x2 = r1.astype(BF16)
    x3 = (r1 - x2.astype(F32)).astype(BF16)
    return _dot(mb, x1, NN) + _dot(mb, x2, NN) + _dot(mb, x3, NN)


def _matmul(a, b, dims, out_dtype, name, tm, tn, tk, a_off=0, m_out=None):
    if dims == NN:
        m, k = a.shape[0], b.shape[0]
        n = b.shape[1]
        a_spec = pl.BlockSpec((tm, tk), lambda i, j, kk: (i, kk + a_off))
        b_spec = pl.BlockSpec((tk, tn), lambda i, j, kk: (kk, j))
    elif dims == NT:
        m, k = a.shape[0], b.shape[1]
        n = b.shape[0]
        a_spec = pl.BlockSpec((tm, tk), lambda i, j, kk: (i, kk + a_off))
        b_spec = pl.BlockSpec((tn, tk), lambda i, j, kk: (j, kk))
    else:
        m, k = (a.shape[1] if m_out is None else m_out), a.shape[0]
        n = b.shape[1]
        a_spec = pl.BlockSpec((tk, tm), lambda i, j, kk: (kk, i + a_off))
        b_spec = pl.BlockSpec((tk, tn), lambda i, j, kk: (kk, j))
    assert m % tm == 0 and n % tn == 0 and k % tk == 0, (name, m, n, k, tm, tn, tk)
    nk = k // tk

    def body(a_ref, b_ref, o_ref, *acc):
        part = _dot(a_ref[...], b_ref[...], dims)
        if nk == 1:
            o_ref[...] = part.astype(o_ref.dtype)
            return
        acc_ref, = acc
        kk = pl.program_id(2)

        @pl.when(kk == 0)
        def _():
            acc_ref[...] = part

        @pl.when(kk > 0)
        def _():
            acc_ref[...] += part

        @pl.when(kk == nk - 1)
        def _():
            o_ref[...] = acc_ref[...].astype(o_ref.dtype)

    return pl.pallas_call(
        body,
        name=name,
        grid=(m // tm, n // tn, nk),
        in_specs=[a_spec, b_spec],
        out_specs=pl.BlockSpec((tm, tn), lambda i, j, kk: (i, j)),
        out_shape=jax.ShapeDtypeStruct((m, n), out_dtype),
        scratch_shapes=[] if nk == 1 else [pltpu.VMEM((tm, tn), F32)],
        compiler_params=_cp("parallel", "parallel", "arbitrary"),
    )(a, b)


def _row(c):
    return pl.BlockSpec((TR, c), lambda i: (i, 0))


def _rowcol(width, cb):
    return pl.BlockSpec((TR, width), lambda i: (i, cb))


def _full(shape):
    return pl.BlockSpec(shape, lambda i: (0,) * len(shape))


def _mod_row(mc_ref, mx_ref, k, is_ctx):
    return jnp.where(is_ctx, mc_ref[k:k + 1, :], mx_ref[k:k + 1, :])


def _acc_row(ref, k, val):
    ref[k:k + 1, :] += val


def _acc_mod(ref, k, is_ctx, val):
    zero = jnp.zeros_like(val)
    ref[k:k + 1, :] += jnp.where(is_ctx, val, zero)
    ref[k + 1:k + 2, :] += jnp.where(is_ctx, zero, val)


def _prenorm(z, nw, modc, modx, i_shift, i_scale, name):
    t = z.shape[0]

    def body(z_ref, nw_ref, mc_ref, mx_ref, h_ref):
        is_ctx = pl.program_id(0) < NCT
        x = z_ref[...]
        n = x * _rstd(x) * nw_ref[...]
        h = n * (1.0 + _mod_row(mc_ref, mx_ref, i_scale, is_ctx)) + _mod_row(mc_ref, mx_ref, i_shift, is_ctx)
        h_ref[...] = h.astype(BF16)

    return pl.pallas_call(
        body, name=name, grid=(t // TR,),
        in_specs=[_row(D), _full((1, D)), _full((8, D)), _full((8, D))],
        out_specs=_row(D),
        out_shape=jax.ShapeDtypeStruct((t, D), BF16),
        compiler_params=_cp("parallel"),
    )(z, nw, modc, modx)


def _hg_lb(lb_ref, d):
    a0 = lb_ref[0, d:d + 1, :]
    a1 = lb_ref[1, d:d + 1, :]
    mx = jnp.maximum(a0, a1)
    e0 = jnp.exp(a0 - mx)
    e1 = jnp.exp(a1 - mx)
    return e0 / (e0 + e1)


def _log_sigmoid(x):
    return jnp.minimum(x, 0.0) - jnp.log(1.0 + jnp.exp(-jnp.abs(x)))


def _gates_fwd(p, hg_lb, wgk, bgk):
    t = p.shape[0]
    seg = lambda j: _rowcol(HW, MAIN0 // HW + j)

    def body(hq_ref, hi_ref, hf_ref, hb_ref, gq_ref, gk_ref, gv_ref, lr_ref, lb_ref, wgk_ref, bgk_ref,
             q_ref, v_ref, kf_ref, kb_ref, gf_ref, gb_ref):
        q_ref[:, :HW] = _silu(hq_ref[...].astype(F32))
        q_ref[:, HW:] = gq_ref[...].astype(F32) * (DH ** -0.5)
        v_ref[:, :HW] = hi_ref[...].astype(F32)
        v_ref[:, HW:] = gv_ref[...].astype(F32)
        xg = _dot(lr_ref[...].astype(BF16), wgk_ref[...], NN) + bgk_ref[...]
        for d, (raw_ref, k_ref, g_ref) in enumerate(((hf_ref, kf_ref, gf_ref), (hb_ref, kb_ref, gb_ref))):
            lbd = _hg_lb(lb_ref, d)
            f = lbd + (1.0 - lbd) * _sig(raw_ref[...].astype(F32))
            k_ref[:, :HW] = 1.0 - f
            k_ref[:, HW:] = gk_ref[...].astype(F32)
            g_ref[:, :HW] = jnp.log(f)
            g_ref[:, HW:] = _log_sigmoid(xg[:, d * HW:(d + 1) * HW]) * (1.0 / GLA_NORM)

    out = jax.ShapeDtypeStruct((t, D), F32)
    return pl.pallas_call(
        body, name="gates_fwd", grid=(t // TR,),
        in_specs=[seg(0), seg(1), seg(2), seg(3), seg(5), seg(6), seg(7), _rowcol(DH, LR0 // DH),
                  _full((2, 2, HW)), _full((DH, D)), _full((1, D))],
        out_specs=[_row(D)] * 6,
        out_shape=[out] * 6,
        compiler_params=_cp("parallel"),
    )(p, p, p, p, p, p, p, p, hg_lb, wgk, bgk)


def _post_fwd(o_fw, o_bw, p, onw):
    t = o_fw.shape[0]

    def body(of_ref, ob_ref, g1_ref, g2_ref, w_ref, y_ref):
        for h in range(NH):
            sl = slice(h * DH, (h + 1) * DH)
            o = of_ref[:, sl] + ob_ref[:, sl]
            g_ref = g1_ref if h < NH // 2 else g2_ref
            gs = slice((h % (NH // 2)) * DH, (h % (NH // 2) + 1) * DH)
            n = o * _rstd(o) * w_ref[:, sl]
            y_ref[:, sl] = (n * _silu(g_ref[:, gs].astype(F32))).astype(BF16)

    return pl.pallas_call(
        body, name="post_fwd", grid=(t // TR,),
        in_specs=[_row(D), _row(D), _rowcol(HW, MAIN0 // HW + 4), _rowcol(HW, MAIN0 // HW + 8), _full((1, D))],
        out_specs=_row(D),
        out_shape=jax.ShapeDtypeStruct((t, D), BF16),
        compiler_params=_cp("parallel"),
    )(o_fw, o_bw, p, p, onw)


def _gate_window_specs(col0):
    return [_rowcol(HW, col0 // HW), _rowcol(HW, col0 // HW + 1), _rowcol(DH, (col0 + 2 * HW) // DH)]


def _gate_window(refs):
    return jnp.concatenate([r[...].astype(F32) for r in refs], axis=1)


def _merge_fwd(p, u1, u2):
    t = p.shape[0]

    def body(a0, a1, a2, b0, b1, b2, u1_ref, u2_ref, m_ref):
        f = lambda r: r[...].astype(F32)
        m_ref[...] = (_sig(_gate_window((a0, a1, a2))) * f(u1_ref)
                      + _sig(_gate_window((b0, b1, b2))) * f(u2_ref)).astype(BF16)

    return pl.pallas_call(
        body, name="merge_fwd", grid=(t // TR,),
        in_specs=_gate_window_specs(GATE_HG0) + _gate_window_specs(GATE_GLA0) + [_row(GW), _row(GW)],
        out_specs=_row(GW),
        out_shape=jax.ShapeDtypeStruct((t, GW), BF16),
        compiler_params=_cp("parallel"),
    )(p, p, p, p, p, p, u1, u2)


def _mid_fwd(z, y1, nw_post, nw_pre, modc, modx):
    t = z.shape[0]

    def body(z_ref, y_ref, wpo_ref, wpr_ref, mc_ref, mx_ref, z1_ref, h_ref):
        is_ctx = pl.program_id(0) < NCT
        y = y_ref[...]
        z1 = z_ref[...] + _mod_row(mc_ref, mx_ref, 2, is_ctx) * (y * _rstd(y) * wpo_ref[...])
        z1_ref[...] = z1
        n = z1 * _rstd(z1) * wpr_ref[...]
        h = n * (1.0 + _mod_row(mc_ref, mx_ref, 4, is_ctx)) + _mod_row(mc_ref, mx_ref, 3, is_ctx)
        h_ref[...] = h.astype(BF16)

    return pl.pallas_call(
        body, name="mid_fwd", grid=(t // TR,),
        in_specs=[_row(D), _row(D), _full((1, D)), _full((1, D)), _full((8, D)), _full((8, D))],
        out_specs=[_row(D), _row(D)],
        out_shape=[jax.ShapeDtypeStruct((t, D), F32), jax.ShapeDtypeStruct((t, D), BF16)],
        compiler_params=_cp("parallel"),
    )(z, y1, nw_post, nw_pre, modc, modx)


def _swiglu_fwd(uv):
    t = uv.shape[0]

    def body(u_ref, v_ref, a_ref):
        a_ref[...] = (_silu(u_ref[...].astype(F32)) * v_ref[...].astype(F32)).astype(BF16)

    return pl.pallas_call(
        body, name="swiglu_fwd", grid=(t // TR,),
        in_specs=[_rowcol(D_FF, 0), _rowcol(D_FF, 1)],
        out_specs=_row(D_FF),
        out_shape=jax.ShapeDtypeStruct((t, D_FF), BF16),
        compiler_params=_cp("parallel"),
    )(uv, uv)


def _swiglu_bwd(uv, da):
    t = uv.shape[0]

    def body(u_ref, v_ref, da_ref, d_ref):
        u = u_ref[...].astype(F32)
        d = da_ref[...].astype(F32)
        d_ref[:, :D_FF] = (d * v_ref[...].astype(F32) * _dsilu(u)).astype(BF16)
        d_ref[:, D_FF:] = (d * _silu(u)).astype(BF16)

    return pl.pallas_call(
        body, name="swiglu_bwd", grid=(t // TR,),
        in_specs=[_rowcol(D_FF, 0), _rowcol(D_FF, 1), _row(D_FF)],
        out_specs=_row(2 * D_FF),
        out_shape=jax.ShapeDtypeStruct((t, 2 * D_FF), BF16),
        compiler_params=_cp("parallel"),
    )(uv, uv, da)


def _final(z1, y2, target, nw, modc, modx):
    t = z1.shape[0]

    def body(z1_ref, y_ref, tg_ref, w_ref, mc_ref, mx_ref, dz_ref, dy_ref, loss_ref, sm_ref):
        i = pl.program_id(0)
        is_ctx = i < NCT

        @pl.when(i == 0)
        def _():
            loss_ref[...] = jnp.zeros_like(loss_ref)
            sm_ref[...] = jnp.zeros_like(sm_ref)

        g = _mod_row(mc_ref, mx_ref, 5, is_ctx)
        y = y_ref[...]
        r = _rstd(y)
        w = w_ref[...]
        yr = y * r
        n = yr * w
        e = z1_ref[...] + g * n - tg_ref[...]
        lat = jnp.where(is_ctx, 0.0, 1.0)
        loss_ref[...] += lat * _colsum(e * e)
        dz = e * (lat / D)
        dz_ref[...] = dz
        _acc_mod(sm_ref, 0, is_ctx, _colsum(dz * n))
        dn = dz * g
        _acc_row(sm_ref, 2, _colsum(dn * yr))
        dy_ref[...] = _rms_bwd(dn * w, y, r).astype(BF16)

    return pl.pallas_call(
        body, name="final", grid=(t // TR,),
        in_specs=[_row(D), _row(D), pl.BlockSpec((TR, D), lambda i: (jnp.maximum(i - NCT, 0), 0)),
                  _full((1, D)), _full((8, D)), _full((8, D))],
        out_specs=[_row(D), _row(D), _full((1, D)), _full((8, D))],
        out_shape=[jax.ShapeDtypeStruct((t, D), F32), jax.ShapeDtypeStruct((t, D), BF16),
                   jax.ShapeDtypeStruct((1, D), F32), jax.ShapeDtypeStruct((8, D), F32)],
        compiler_params=_cp("arbitrary"),
    )(z1, y2, target, nw, modc, modx)


def _mid_bwd(dh2, dz, z, z1, y1, nw_post, nw_pre, modc, modx):
    t = z.shape[0]

    def body(dh_ref, dz_ref, z_ref, z1_ref, y_ref, wpo_ref, wpr_ref, mc_ref, mx_ref, dzo_ref, dy_ref, sm_ref):
        i = pl.program_id(0)
        is_ctx = i < NCT

        @pl.when(i == 0)
        def _():
            sm_ref[...] = jnp.zeros_like(sm_ref)

        dh = dh_ref[...]
        z1 = z1_ref[...]
        r = _rstd(z1)
        zr = z1 * r
        wpr = wpr_ref[...]
        n = zr * wpr
        _acc_mod(sm_ref, 0, is_ctx, _colsum(dh))
        _acc_mod(sm_ref, 2, is_ctx, _colsum(dh * n))
        dn = dh * (1.0 + _mod_row(mc_ref, mx_ref, 4, is_ctx))
        _acc_row(sm_ref, 6, _colsum(dn * zr))
        dz1 = dz_ref[...] + _rms_bwd(dn * wpr, z1, r)
        dzo_ref[...] = dz1
        y = y_ref[...]
        r1 = _rstd(y)
        yr = y * r1
        wpo = wpo_ref[...]
        g = _mod_row(mc_ref, mx_ref, 2, is_ctx)
        _acc_mod(sm_ref, 4, is_ctx, _colsum(dz1 * (yr * wpo)))
        dn1 = dz1 * g
        _acc_row(sm_ref, 7, _colsum(dn1 * yr))
        dy_ref[...] = _rms_bwd(dn1 * wpo, y, r1).astype(BF16)

    return pl.pallas_call(
        body, name="mid_bwd", grid=(t // TR,),
        in_specs=[_row(D)] * 5 + [_full((1, D)), _full((1, D)), _full((8, D)), _full((8, D))],
        out_specs=[_row(D), _row(D), _full((8, D))],
        out_shape=[jax.ShapeDtypeStruct((t, D), F32), jax.ShapeDtypeStruct((t, D), BF16),
                   jax.ShapeDtypeStruct((8, D), F32)],
        compiler_params=_cp("arbitrary"),
    )(dh2, dz, z, z1, y1, nw_post, nw_pre, modc, modx)


def _pre_bwd(dh1, dz, z, nw, modc, modx):
    t = z.shape[0]

    def body(dh_ref, dz_ref, z_ref, w_ref, mc_ref, mx_ref, dzo_ref, sm_ref):
        i = pl.program_id(0)
        is_ctx = i < NCT

        @pl.when(i == 0)
        def _():
            sm_ref[...] = jnp.zeros_like(sm_ref)

        dh = dh_ref[...]
        x = z_ref[...]
        r = _rstd(x)
        xr = x * r
        w = w_ref[...]
        _acc_mod(sm_ref, 0, is_ctx, _colsum(dh))
        _acc_mod(sm_ref, 2, is_ctx, _colsum(dh * (xr * w)))
        dn = dh * (1.0 + _mod_row(mc_ref, mx_ref, 1, is_ctx))
        _acc_row(sm_ref, 4, _colsum(dn * xr))
        dzo_ref[...] = dz_ref[...] + _rms_bwd(dn * w, x, r)

    return pl.pallas_call(
        body, name="pre_bwd", grid=(t // TR,),
        in_specs=[_row(D)] * 3 + [_full((1, D)), _full((8, D)), _full((8, D))],
        out_specs=[pl.BlockSpec((TR, D), lambda i: (jnp.maximum(i - NCT, 0), 0)), _full((8, D))],
        out_shape=[jax.ShapeDtypeStruct((t - CTX, D), F32), jax.ShapeDtypeStruct((8, D), F32)],
        compiler_params=_cp("arbitrary"),
    )(dh1, dz, z, nw, modc, modx)


def _merge_bwd(dm, p, u1, u2):
    t = dm.shape[0]

    def body(dm_ref, a0, a1, a2, b0, b1, b2, u1_ref, u2_ref, du1_ref, du2_ref, dg_ref):
        dm_ = dm_ref[...]
        s1 = _sig(_gate_window((a0, a1, a2)))
        s2 = _sig(_gate_window((b0, b1, b2)))
        du1_ref[...] = (dm_ * s1).astype(BF16)
        du2_ref[...] = (dm_ * s2).astype(BF16)
        dg_ref[:, :GW] = (dm_ * u1_ref[...].astype(F32) * s1 * (1.0 - s1)).astype(BF16)
        dg_ref[:, GW:] = (dm_ * u2_ref[...].astype(F32) * s2 * (1.0 - s2)).astype(BF16)

    return pl.pallas_call(
        body, name="merge_bwd", grid=(t // TR,),
        in_specs=[_row(GW)] + _gate_window_specs(GATE_HG0) + _gate_window_specs(GATE_GLA0) + [_row(GW), _row(GW)],
        out_specs=[_row(GW), _row(GW), _row(2 * GW)],
        out_shape=[jax.ShapeDtypeStruct((t, GW), BF16), jax.ShapeDtypeStruct((t, GW), BF16),
                   jax.ShapeDtypeStruct((t, 2 * GW), BF16)],
        compiler_params=_cp("parallel"),
    )(dm, p, p, p, p, p, p, u1, u2)


def _post_bwd(dy_hg, dy_gla, o_fw, o_bw, p, onw):
    t = o_fw.shape[0]

    def body(d1_ref, d2_ref, of_ref, ob_ref, g1_ref, g2_ref, w_ref, do_ref, dg_ref, sm_ref):
        @pl.when(pl.program_id(0) == 0)
        def _():
            sm_ref[...] = jnp.zeros_like(sm_ref)

        for h in range(NH):
            sl = slice(h * DH, (h + 1) * DH)
            gs = slice((h % (NH // 2)) * DH, (h % (NH // 2) + 1) * DH)
            g_ref, d_ref = (g1_ref, d1_ref) if h < NH // 2 else (g2_ref, d2_ref)
            o = of_ref[:, sl] + ob_ref[:, sl]
            r = _rstd(o)
            orr = o * r
            w = w_ref[:, sl]
            gt = g_ref[:, gs].astype(F32)
            dy = d_ref[:, gs]
            dg_ref[:, sl] = (dy * (orr * w) * _dsilu(gt)).astype(BF16)
            dn = dy * _silu(gt)
            sm_ref[0:1, sl] += _colsum(dn * orr)
            do_ref[:, sl] = _rms_bwd(dn * w, o, r)

    return pl.pallas_call(
        body, name="post_bwd", grid=(t // TR,),
        in_specs=[_row(HW), _row(HW), _row(D), _row(D), _rowcol(HW, MAIN0 // HW + 4), _rowcol(HW, MAIN0 // HW + 8),
                  _full((1, D))],
        out_specs=[_row(D), _row(D), _full((8, D))],
        out_shape=[jax.ShapeDtypeStruct((t, D), F32), jax.ShapeDtypeStruct((t, D), BF16),
                   jax.ShapeDtypeStruct((8, D), F32)],
        compiler_params=_cp("arbitrary"),
    )(dy_hg, dy_gla, o_fw, o_bw, p, p, onw)


def _gates_bwd(p, hg_lb, wgk, bgk, dgm, dgo, dq_f, dq_b, dv_f, dv_b, dk_f, dk_b, dg_f, dg_b):
    t = p.shape[0]
    seg = lambda j: _rowcol(HW, MAIN0 // HW + j)

    def body(hq_ref, hf_ref, hb_ref, lr_ref, lb_ref, wgk_ref, bgk_ref, dgm_ref, dgo_ref,
             dqf_ref, dqb_ref, dvf_ref, dvb_ref, dkf_ref, dkb_ref, dgf_ref, dgb_ref,
             dp_ref, dlb_ref, dw_ref, db_ref):
        @pl.when(pl.program_id(0) == 0)
        def _():
            dlb_ref[...] = jnp.zeros_like(dlb_ref)
            dw_ref[...] = jnp.zeros_like(dw_ref)
            db_ref[...] = jnp.zeros_like(db_ref)

        c0 = MAIN0

        def put(j, val):
            dp_ref[:, c0 + j * HW:c0 + (j + 1) * HW] = val.astype(BF16)

        dq = dqf_ref[...] + dqb_ref[...]
        dv = dvf_ref[...] + dvb_ref[...]
        put(0, dq[:, :HW] * _dsilu(hq_ref[...].astype(F32)))
        put(1, dv[:, :HW])
        put(5, dq[:, HW:] * (DH ** -0.5))
        put(7, dv[:, HW:])
        put(6, dkf_ref[:, HW:] + dkb_ref[:, HW:])
        dp_ref[:, c0 + 4 * HW:c0 + 5 * HW] = dgo_ref[:, :HW]
        dp_ref[:, c0 + 8 * HW:c0 + 9 * HW] = dgo_ref[:, HW:]
        lr = lr_ref[...].astype(BF16)
        xg = _dot(lr, wgk_ref[...], NN) + bgk_ref[...]
        dxg = []
        for d, (raw_ref, dk_ref, dg_ref) in enumerate(((hf_ref, dkf_ref, dgf_ref), (hb_ref, dkb_ref, dgb_ref))):
            lbd = _hg_lb(lb_ref, d)
            s = _sig(raw_ref[...].astype(F32))
            f = lbd + (1.0 - lbd) * s
            df = dg_ref[:, :HW] / f - dk_ref[:, :HW]
            put(2 + d, df * (1.0 - lbd) * s * (1.0 - s))
            dlb_ref[d:d + 1, :] += _colsum(df * (1.0 - s)) * (lbd * (1.0 - lbd))
            dxg.append(dg_ref[:, HW:] * (1.0 / GLA_NORM) * _sig(-xg[:, d * HW:(d + 1) * HW]))
        dxg = jnp.concatenate(dxg, axis=1)
        db_ref[0:1, :] += _colsum(dxg)
        dxg_b = dxg.astype(BF16)
        dw_ref[...] += _dot(lr, dxg_b, TN)
        dlr = _dot(dxg_b, wgk_ref[...], NT)
        dp_ref[:, LR0:LR0 + DH] = (dlr + dgm_ref[:, :DH].astype(F32)).astype(BF16)
        dp_ref[:, LR0 + DH:GATE_GLA0] = dgm_ref[:, DH:D]
        dp_ref[:, GATE_GLA0:GATE_GLA0 + DH] = dgm_ref[:, D:GW] + dgm_ref[:, GW:GW + DH]
        dp_ref[:, GATE_GLA0 + DH:GATE_GLA0 + GW] = dgm_ref[:, GW + DH:]
        dp_ref[:, GATE_GLA0 + GW:] = jnp.zeros((TR, W_IN_COLS - GATE_GLA0 - GW), BF16)

    return pl.pallas_call(
        body, name="gates_bwd", grid=(t // TR,),
        in_specs=[seg(0), seg(2), seg(3), _rowcol(DH, LR0 // DH), _full((2, 2, HW)), _full((DH, D)), _full((1, D)),
                  _row(2 * GW), _row(D)] + [_row(D)] * 8,
        out_specs=[_row(W_IN_COLS), _full((8, HW)), _full((DH, D)), _full((8, D))],
        out_shape=[jax.ShapeDtypeStruct((t, W_IN_COLS), BF16), jax.ShapeDtypeStruct((8, HW), F32),
                   jax.ShapeDtypeStruct((DH, D), F32), jax.ShapeDtypeStruct((8, D), F32)],
        compiler_params=_cp("arbitrary"),
    )(p, p, p, p, hg_lb, wgk, bgk, dgm, dgo, dq_f, dq_b, dv_f, dv_b, dk_f, dk_b, dg_f, dg_b)


def _scan_consts(rev):
    r = lax.broadcasted_iota(jnp.int32, (CHUNK, CHUNK), 0)
    u = lax.broadcasted_iota(jnp.int32, (CHUNK, CHUNK), 1)
    rp = lax.broadcasted_iota(jnp.int32, (CHUNK, 1), 0)
    if rev:
        r, u, rp = CHUNK - 1 - r, CHUNK - 1 - u, CHUNK - 1 - rp
    tri = jnp.where(u <= r, 1.0, 0.0).astype(F32)
    tri_t = jnp.where(r <= u, 1.0, 0.0).astype(F32)
    lv = []
    for b in LEVELS:
        sh = b.bit_length() - 1
        pair = ((r >> sh) == (u >> sh) + 1) & (((u >> sh) & 1) == 0)
        pair_t = ((u >> sh) == (r >> sh) + 1) & (((r >> sh) & 1) == 0)
        tside = ((rp >> sh) & 1) == 1
        lv.append((pair, pair_t, tside))
    bd = LEVELS[-1].bit_length() - 1
    diag = ((r >> bd) == (u >> bd)) & (u <= r)
    diag_t = ((r >> bd) == (u >> bd)) & (r <= u)
    return tri, tri_t, lv, diag, diag_t


def _row_of(pos, rev):
    return CHUNK - 1 - pos if rev else pos


def _chunk_terms(cum, b_scr, consts, rev):
    _, _, lv, _, _ = consts
    terms = []
    for b, (_, _, tside) in zip(LEVELS, lv):
        pieces = []
        for j in range(CHUNK // (2 * b)):
            row = _row_of(2 * b * j + b - 1, rev)
            pieces.append(jnp.broadcast_to(b_scr[row:row + 1, :], (2 * b, DH)))
        if rev:
            pieces = pieces[::-1]
        bnd = pieces[0] if len(pieces) == 1 else jnp.concatenate(pieces, axis=0)
        w = jnp.exp(jnp.minimum(jnp.where(tside, cum - bnd, bnd - cum), 0.0))
        wq = jnp.where(tside, w, 0.0)
        wk = jnp.where(tside, 0.0, w)
        terms.append((wq, wk))
    b = LEVELS[-1]
    pieces = []
    for j in range(CHUNK // b):
        if j == 0:
            pieces.append(jnp.zeros((b, DH), F32))
        else:
            row = _row_of(b * j - 1, rev)
            pieces.append(jnp.broadcast_to(b_scr[row:row + 1, :], (b, DH)))
    if rev:
        pieces = pieces[::-1]
    start = jnp.concatenate(pieces, axis=0)
    wq = jnp.exp(jnp.minimum(cum - start, 0.0))
    wk = jnp.exp(jnp.minimum(start - cum, EXP_CLAMP))
    terms.append((wq, wk))
    return terms


def _run_staged(units):
    live = list(units)
    while live:
        nxt = []
        for u in live:
            try:
                next(u)
                nxt.append(u)
            except StopIteration:
                pass
        live = nxt


SCAN_TB = 256
SCAN_CB = SCAN_TB // CHUNK


def _block_order(i, ntb, rev):
    nctx = CTX // SCAN_TB
    if not rev:
        return i
    return jnp.where(i < nctx, nctx - 1 - i, ntb - 1 - (i - nctx))


def _chunk_in_block(j, rev):
    return SCAN_CB - 1 - j if rev else j


def _scan_fwd(q, k, v, g, rev):
    t = q.shape[0]
    nc = t // CHUNK
    hpb = SCAN_HEADS_FWD

    def body(q_ref, k_ref, v_ref, g_ref, o_ref, st_ref, s_scr, b_scr):
        consts = _scan_consts(rev)
        _, _, lv, diag, _ = consts
        masks = [pair for pair, _, _ in lv] + [diag]

        @pl.when(pl.program_id(1) == 0)
        def _():
            s_scr[...] = jnp.zeros_like(s_scr)

        tri = consts[0]
        state = {hh: s_scr[hh] for hh in range(hpb)}

        def unit(hh, j):
            sl = slice(hh * DH, (hh + 1) * DH)
            c = _chunk_in_block(j, rev)
            rows = slice(c * CHUNK, (c + 1) * CHUNK)
            b_ref = b_scr.at[hh * SCAN_CB + j]
            qc, kc, vc, gc = q_ref[rows, sl], k_ref[rows, sl], v_ref[rows, sl], g_ref[rows, sl]
            cum = _split_dot(tri, gc)
            b_ref[...] = cum
            yield
            terms = _chunk_terms(cum, b_ref, consts, rev)
            ops = [((qc * wq).astype(BF16), (kc * wk).astype(BF16)) for wq, wk in terms]
            tot = _colsum(gc)
            qe = (qc * jnp.exp(cum)).astype(BF16)
            ke = (kc * jnp.exp(tot - cum)).astype(BF16)
            vb = vc.astype(BF16)
            yield
            scs = [_dot(qt, kt, NT) for qt, kt in ops]
            kv = _dot(vb, ke, TN)
            yield
            a = jnp.zeros((CHUNK, CHUNK), F32)
            for sc, m in zip(scs, masks):
                a = a + jnp.where(m, sc, 0.0)
            o_intra = _dot(a.astype(BF16), vb, NN)
            yield
            st = state[hh]
            st_ref[hh, c] = st
            o_ref[rows, sl] = o_intra + _dot(qe, st.astype(BF16), NT)
            state[hh] = st * jnp.exp(tot) + kv
            yield

        _run_staged([unit(hh, j) for hh in range(hpb) for j in range(SCAN_CB)])
        for hh in range(hpb):
            s_scr[hh] = state[hh]

    ntb = t // SCAN_TB
    col = pl.BlockSpec((SCAN_TB, hpb * DH), lambda h, i: (_block_order(i, ntb, rev), h))
    return pl.pallas_call(
        body, name="scan_fwd_" + ("bw" if rev else "fw"), grid=(NH // hpb, ntb),
        in_specs=[col] * 4,
        out_specs=[col, pl.BlockSpec((hpb, SCAN_CB, DH, DH), lambda h, i: (h, _block_order(i, ntb, rev), 0, 0))],
        out_shape=[jax.ShapeDtypeStruct((t, D), F32), jax.ShapeDtypeStruct((NH, nc, DH, DH), F32)],
        scratch_shapes=[pltpu.VMEM((hpb, DH, DH), F32), pltpu.VMEM((hpb * SCAN_CB, CHUNK, DH), F32)],
        compiler_params=_cp("parallel", "arbitrary"),
    )(q, k, v, g)


def _scan_bwd(q, k, v, g, do, states, rev):
    t = q.shape[0]
    nc = t // CHUNK
    hpb = SCAN_HEADS_BWD

    def body(q_ref, k_ref, v_ref, g_ref, do_ref, st_ref, dq_ref, dk_ref, dv_ref, dg_ref, ds_scr, b_scr):
        consts = _scan_consts(rev)
        _, tri_t, lv, diag, diag_t = consts
        masks = [(pair, pair_t) for pair, pair_t, _ in lv] + [(diag, diag_t)]
        @pl.when(pl.program_id(1) == 0)
        def _():
            ds_scr[...] = jnp.zeros_like(ds_scr)

        tri = consts[0]
        dstate = {hh: ds_scr[hh] for hh in range(hpb)}

        def unit(hh, jj):
            sl = slice(hh * DH, (hh + 1) * DH)
            c = _chunk_in_block(SCAN_CB - 1 - jj, rev)
            rows = slice(c * CHUNK, (c + 1) * CHUNK)
            b_ref = b_scr.at[hh * SCAN_CB + jj]
            qc, kc, vc, gc = q_ref[rows, sl], k_ref[rows, sl], v_ref[rows, sl], g_ref[rows, sl]
            dob = do_ref[rows, sl].astype(BF16)
            vb = vc.astype(BF16)
            cum = _split_dot(tri, gc)
            b_ref[...] = cum
            da = _dot(dob, vb, NT)
            da_t = _dot(vb, dob, NT)
            yield
            terms = _chunk_terms(cum, b_ref, consts, rev)
            ops = [((qc * wq).astype(BF16), (kc * wk).astype(BF16)) for wq, wk in terms]
            tot = _colsum(gc)
            e_tot = jnp.exp(tot)
            e_b = jnp.exp(cum)
            e_t = jnp.exp(tot - cum)
            qeb = (qc * e_b).astype(BF16)
            keb = (kc * e_t).astype(BF16)
            dal = [(jnp.where(m, da, 0.0).astype(BF16), jnp.where(m_t, da_t, 0.0).astype(BF16)) for m, m_t in masks]
            yield
            ats = [_dot(ktb, qtb, NT) for qtb, ktb in ops]
            dqts = [_dot(d, ktb, NN) for (d, _), (_, ktb) in zip(dal, ops)]
            dkts = [_dot(d_t, qtb, NN) for (_, d_t), (qtb, _) in zip(dal, ops)]
            qd = _dot(dob, qeb, TN)
            yield
            a_t = jnp.zeros((CHUNK, CHUNK), F32)
            dq = jnp.zeros((CHUNK, DH), F32)
            dk = jnp.zeros((CHUNK, DH), F32)
            db = jnp.zeros((CHUNK, DH), F32)
            for at, dqt, dkt, (wq, wk), (qtb, ktb), (_, m_t) in zip(ats, dqts, dkts, terms, ops, masks):
                a_t = a_t + jnp.where(m_t, at, 0.0)
                dq = dq + dqt * wq
                dk = dk + dkt * wk
                db = db + dqt * qtb.astype(F32) - dkt * ktb.astype(F32)
            dv_intra = _dot(a_t.astype(BF16), dob, NN)
            st = st_ref[hh, c]
            stb = st.astype(BF16)
            dqe = _dot(dob, stb, NN)
            yield
            dst = dstate[hh]
            dstb = dst.astype(BF16)
            dstate[hh] = dst * e_tot + qd
            dv_ref[rows, sl] = dv_intra + _dot(keb, dstb, NT)
            dke = _dot(vb, dstb, NN)
            yield
            qe = qeb.astype(F32)
            ke = keb.astype(F32)
            dq_ref[rows, sl] = dq + dqe * e_b
            dk_ref[rows, sl] = dk + dke * e_t
            db = db + dqe * qe - dke * ke
            dtot = _colsum(dstb.astype(F32) * stb.astype(F32)) * e_tot + _colsum(dke * ke)
            dg_ref[rows, sl] = _split_dot(tri_t, db) + dtot
            yield

        _run_staged([unit(hh, jj) for hh in range(hpb) for jj in range(SCAN_CB)])
        for hh in range(hpb):
            ds_scr[hh] = dstate[hh]

    ntb = t // SCAN_TB
    blk = lambda i: _block_order(ntb - 1 - i, ntb, rev)
    col = pl.BlockSpec((SCAN_TB, hpb * DH), lambda h, i: (blk(i), h))
    out = jax.ShapeDtypeStruct((t, D), F32)
    return pl.pallas_call(
        body, name="scan_bwd_" + ("bw" if rev else "fw"), grid=(NH // hpb, ntb),
        in_specs=[col] * 5 + [pl.BlockSpec((hpb, SCAN_CB, DH, DH), lambda h, i: (h, blk(i), 0, 0))],
        out_specs=[col] * 4,
        out_shape=[out] * 4,
        scratch_shapes=[pltpu.VMEM((hpb, DH, DH), F32), pltpu.VMEM((hpb * SCAN_CB, CHUNK, DH), F32)],
        compiler_params=_cp("parallel", "arbitrary"),
    )(q, k, v, g, do, states)


W_IN_GRAD_CHUNKS = (("a", (0, 512)), ("b", (0, 96)), ("b", (96, 512)))
W_IN_REF = 6688


def _layout_w_in(w):
    return jnp.pad(w, ((0, 0), (0, W_IN_COLS - W_IN_REF)))


def _unlayout_w_in(d):
    return d[:, :W_IN_REF]


def _gate_cols(w):
    return jnp.pad(w, ((0, 0), (GOFF, GW - GOFF - D)))


def _gate_rows(w):
    return jnp.pad(w, ((GOFF, GW - GOFF - D), (0, 0)))


def _layout_wgk(w):
    r = w.shape[1]
    top = jnp.concatenate([w[0], jnp.zeros_like(w[0])], axis=1)
    bot = jnp.concatenate([jnp.zeros_like(w[1]), w[1]], axis=1)
    return jnp.concatenate([top, bot, jnp.zeros((DH - 2 * r, D), w.dtype)], axis=0)


def _unlayout_wgk(d, r=16):
    return jnp.stack([d[:r, :HW], d[r:2 * r, HW:]])


def _local_step(z, target, modc, modx, norms, onw, hg_lb, wgk, bgk, w_in, get_mix, get_ffn, send):
    n_pre1, n_post1, n_pre2, n_post2 = norms
    t = z.shape[0]
    tm = 768 if t % 768 == 0 else 256
    h1 = _prenorm(z, n_pre1, modc, modx, 0, 1, "prenorm1")
    p = _matmul(h1, w_in, NN, BF16, "mm_in", tm, 512, D)
    q, v, k_f, k_b, g_f, g_b = _gates_fwd(p, hg_lb, wgk, bgk)
    o_f, st_f = _scan_fwd(q, k_f, v, g_f, False)
    o_b, st_b = _scan_fwd(q, k_b, v, g_b, True)
    y = _post_fwd(o_f, o_b, p, onw)
    w_br_hg, w_br_gla, w_out = get_mix(y)
    u1 = _matmul(y, w_br_hg, NN, BF16, "mm_br_hg", tm, GW, HW, a_off=0)
    u2 = _matmul(y, w_br_gla, NN, BF16, "mm_br_gla", tm, GW, HW, a_off=1)
    merged = _merge_fwd(p, u1, u2)
    y1 = _matmul(merged, w_out, NN, F32, "mm_out", tm, 512, GW)
    z1, h2 = _mid_fwd(z, y1, n_post1, n_pre2, modc, modx)
    w_gu_t, w_down = get_ffn(h2)
    uv = _matmul(h2, w_gu_t, NT, BF16, "mm_gu", tm, 512, D)
    act = _swiglu_fwd(uv)
    y2 = _matmul(act, w_down, NN, F32, "mm_down", tm, 512, D_FF // 2)
    dz, dy2, loss_vec, sm_final = _final(z1, y2, target, n_post2, modc, modx)
    dact = _matmul(dy2, w_down, NT, BF16, "mm_down_dx", tm, D_FF // 2, D)
    d_w_down = _matmul(act, dy2, TN, BF16, "mm_down_dw", D_FF // 2, 512, t)
    duv = _swiglu_bwd(uv, dact)
    dh2 = _matmul(duv, w_gu_t, NN, F32, "mm_gu_dx", tm, 512, D_FF // 2)
    d_w_gu_t = _matmul(duv, h2, TN, BF16, "mm_gu_dw", 512, 512, t)
    dh2 = send(("w_down", "w_gu_t"), (d_w_down, d_w_gu_t), dh2)
    dz, dy1, sm_mid = _mid_bwd(dh2, dz, z, z1, y1, n_post1, n_pre2, modc, modx)
    dmerged = _matmul(dy1, w_out, NT, F32, "mm_out_dx", tm, GW, D)
    d_w_out = _matmul(merged, dy1, TN, BF16, "mm_out_dw", GW, 512, t)
    du1, du2, dgm = _merge_bwd(dmerged, p, u1, u2)
    dy_hg = _matmul(du1, w_br_hg, NT, F32, "mm_br_hg_dx", tm, HW, GW)
    dy_gla = _matmul(du2, w_br_gla, NT, F32, "mm_br_gla_dx", tm, HW, GW)
    d_w_br_hg = _matmul(y, du1, TN, BF16, "mm_br_hg_dw", HW, GW, t, a_off=0, m_out=HW)
    d_w_br_gla = _matmul(y, du2, TN, BF16, "mm_br_gla_dw", HW, GW, t, a_off=1, m_out=HW)
    dy_hg = send(("w_out", "w_br_hg", "w_br_gla"), (d_w_out, d_w_br_hg, d_w_br_gla), dy_hg)
    do, dgo, sm_post = _post_bwd(dy_hg, dy_gla, o_f, o_b, p, onw)
    dq_f, dk_f, dv_f, dg_f = _scan_bwd(q, k_f, v, g_f, do, st_f, False)
    dq_b, dk_b, dv_b, dg_b = _scan_bwd(q, k_b, v, g_b, do, st_b, True)
    dp, d_lb, d_wgk, d_bgk = _gates_bwd(p, hg_lb, wgk, bgk, dgm, dgo, dq_f, dq_b, dv_f, dv_b, dk_f, dk_b, dg_f, dg_b)
    d_w_in_a = _matmul(h1, dp, TN, BF16, "mm_in_dw_a", 512, 512, t, a_off=0, m_out=D // 2)
    dp = send(("w_in_a",), (d_w_in_a,), dp)
    d_w_in_b = _matmul(h1, dp, TN, BF16, "mm_in_dw_b", 512, 512, t, a_off=1, m_out=D // 2)
    dp = send(("w_in_b",), (d_w_in_b,), dp)
    dh1 = _matmul(dp, w_in, NT, F32, "mm_in_dx", tm, 512, 1024)
    grad_x, sm_pre = _pre_bwd(dh1, dz, z, n_pre1, modc, modx)
    return dict(loss_vec=loss_vec, grad_x=grad_x, sm_final=sm_final, sm_mid=sm_mid, sm_post=sm_post, sm_pre=sm_pre,
                d_lb=d_lb, d_wgk=d_wgk, d_bgk=d_bgk)


MESH = pl.DeviceIdType.MESH
ANY = pl.BlockSpec(memory_space=pl.ANY)
N_REL = N_DEV - 1


def _place():
    return lax.axis_index("x"), lax.axis_index("y"), lax.axis_index("c")


def _slot(p):
    return 4 * p[0] + 2 * p[1] + p[2]


def _all_gather(arrays, name):
    n = len(arrays)

    def body(*refs):
        ins, outs = refs[:n], refs[n:2 * n]
        send_sems, recv_sems, local_sems = refs[2 * n:]
        x, y, c = _place()
        me, sibling = (x, y, c), (x, y, 1 - c)
        chips = [(1 - x, y), (x, 1 - y), (1 - x, 1 - y)]

        def copy(a, k, block, to, src=None):
            dst = outs[a].at[_slot(block)]
            return pltpu.make_async_remote_copy(
                src_ref=dst if src is None else src, dst_ref=dst,
                send_sem=send_sems.at[N_REL * a + k], recv_sem=recv_sems.at[N_REL * a + k],
                device_id=to, device_id_type=MESH)

        mine = [pltpu.make_async_copy(ins[a], outs[a].at[_slot(me)], local_sems.at[a]) for a in range(n)]
        for cp in mine:
            cp.start()
        first = []
        for a in range(n):
            first.append(copy(a, 0, me, sibling, src=ins[a]))
            first += [copy(a, 1 + j, me, (*chip, c), src=ins[a]) for j, chip in enumerate(chips)]
        for cp in first:
            cp.start()
        passed = []
        for j, chip in enumerate(chips):
            for a in range(n):
                copy(a, 1 + j, (*chip, c), me).wait_recv()
                fwd = copy(a, 4 + j, (*chip, c), sibling)
                fwd.start()
                passed.append(fwd)
        for a in range(n):
            copy(a, 0, sibling, me).wait_recv()
        for j, chip in enumerate(chips):
            for a in range(n):
                copy(a, 4 + j, (*chip, 1 - c), me).wait_recv()
        for cp in first + passed:
            cp.wait_send()
        for cp in mine:
            cp.wait()

    return pl.pallas_call(
        body, name=name,
        in_specs=[ANY] * n, out_specs=[ANY] * n,
        out_shape=[jax.ShapeDtypeStruct((N_DEV,) + a.shape, a.dtype) for a in arrays],
        scratch_shapes=[pltpu.SemaphoreType.DMA((N_REL * n,)), pltpu.SemaphoreType.DMA((N_REL * n,)),
                        pltpu.SemaphoreType.DMA((n,))],
    )(*arrays)


def _exchange(arrays, name):
    n = len(arrays)

    def body(*refs):
        ins, outs = refs[:n], refs[n:2 * n]
        send_sems, recv_sems, local_sems = refs[2 * n:]
        x, y, c = _place()
        me = _slot((x, y, c))
        mine = [pltpu.make_async_copy(ins[a].at[me], outs[a].at[me], local_sems.at[a]) for a in range(n)]
        for cp in mine:
            cp.start()
        copies = []
        for a in range(n):
            for k in range(1, N_DEV):
                flip = lambda v, bit: 1 - v if bit else v
                peer = (flip(x, k & 4), flip(y, k & 2), flip(c, k & 1))
                copies.append(pltpu.make_async_remote_copy(
                    src_ref=ins[a].at[_slot(peer)], dst_ref=outs[a].at[me],
                    send_sem=send_sems.at[N_REL * a + k - 1], recv_sem=recv_sems.at[N_REL * a + k - 1],
                    device_id=peer, device_id_type=MESH))
                copies[-1].start()
        i = 0
        for a in range(n):
            for k in range(1, N_DEV):
                flip = lambda v, bit: 1 - v if bit else v
                peer = (flip(x, k & 4), flip(y, k & 2), flip(c, k & 1))
                pltpu.make_async_remote_copy(
                    src_ref=ins[a].at[_slot(peer)], dst_ref=outs[a].at[_slot(peer)],
                    send_sem=send_sems.at[N_REL * a + k - 1], recv_sem=recv_sems.at[N_REL * a + k - 1],
                    device_id=peer, device_id_type=MESH).wait_recv()
                i += 1
        for cp in copies:
            cp.wait_send()
        for cp in mine:
            cp.wait()

    return pl.pallas_call(
        body, name=name,
        in_specs=[ANY] * n, out_specs=[ANY] * n,
        out_shape=[jax.ShapeDtypeStruct(a.shape, a.dtype) for a in arrays],
        scratch_shapes=[pltpu.SemaphoreType.DMA((N_REL * n,)), pltpu.SemaphoreType.DMA((N_REL * n,)),
                        pltpu.SemaphoreType.DMA((n,))],
    )(*arrays)


HBM = pl.BlockSpec(memory_space=pltpu.HBM)
SEM = pl.BlockSpec(memory_space=pltpu.SEMAPHORE)
EFFECT = pltpu.SideEffectType.DATAFLOW_SIDE_EFFECTING


def _peer_of(x, y, c, k):
    flip = lambda v, bit: 1 - v if bit else v
    return flip(x, k & 4), flip(y, k & 2), flip(c, k & 1)


def _view_whole(src, slot):
    return src


def _view_block(src, slot):
    return src.at[slot]


W_IN_SHARD = W_IN_REF // N_DEV


def _view_window(rows):
    def view(src, slot):
        col0 = pl.multiple_of((W_IN_SHARD * slot // DH) * DH, DH)
        return src.at[pl.ds(rows[0], rows[1] - rows[0]), pl.ds(col0, D)]
    return view


def _split_copies(view, srcs, lands, send_sems, recv_sems, local_sems):
    x, y, c = _place()
    me = _slot((x, y, c))
    local, sends, waits = [], [], []
    for a, (src, land) in enumerate(zip(srcs, lands)):
        local.append(pltpu.make_async_copy(view(src, me), land.at[me], local_sems.at[a]))
        for k in range(1, N_DEV):
            peer = _peer_of(x, y, c, k)
            mine = view(src, _slot(peer))
            sems = dict(send_sem=send_sems.at[N_REL * a + k - 1], recv_sem=recv_sems.at[N_REL * a + k - 1],
                        device_id=peer, device_id_type=MESH)
            sends.append(pltpu.make_async_remote_copy(src_ref=mine, dst_ref=land.at[me], **sems))
            waits.append(pltpu.make_async_remote_copy(src_ref=mine, dst_ref=land.at[_slot(peer)], **sems))
    return local, sends, waits


def _split_start(view, land_shapes, srcs, name, after):
    n = len(srcs)
    lands = [lax.empty(shp, s.dtype) for shp, s in zip(land_shapes, srcs)]

    def body(*refs):
        src_refs, land_refs = refs[:n], refs[n:2 * n]
        send_sems, recv_sems, local_sems = refs[2 * n + 1:2 * n + 4]
        token = refs[-1]
        local, sends, _ = _split_copies(view, src_refs, land_refs, send_sems, recv_sems, local_sems)
        for cp in local + sends:
            cp.start()
        token[...] = jnp.zeros_like(token)

    hbm = lambda a: pltpu.with_memory_space_constraint(a, pltpu.HBM)
    out = pl.pallas_call(
        body, name=name,
        out_shape=(pltpu.SemaphoreType.DMA((N_REL * n,)), pltpu.SemaphoreType.DMA((N_REL * n,)),
                   pltpu.SemaphoreType.DMA((n,)),
                   *[pltpu.HBM(s.shape, s.dtype) for s in srcs], *[pltpu.HBM(l.shape, l.dtype) for l in lands],
                   jax.ShapeDtypeStruct((8, DH), F32)),
        in_specs=[HBM] * (2 * n) + [ANY],
        out_specs=(SEM, SEM, SEM, *([HBM] * (2 * n)), pl.BlockSpec(memory_space=pltpu.VMEM)),
        input_output_aliases={i: 3 + i for i in range(2 * n)},
        compiler_params=pltpu.CompilerParams(has_side_effects=EFFECT),
    )(*[hbm(s) for s in srcs], *[hbm(l) for l in lands], after)
    handle = dict(view=view, n=n, sems=out[:3], srcs=list(out[3:3 + n]), lands=list(out[3 + n:3 + 2 * n]))
    return handle, out[-1]


def _split_wait(handle, name, after, srcs=None):
    view, n, sems, lands = handle["view"], handle["n"], handle["sems"], handle["lands"]
    srcs = handle["srcs"] if srcs is None else srcs
    afters = list(after) if isinstance(after, (list, tuple)) else [after]

    def body(*refs):
        src_refs, land_refs = refs[:n], refs[n:2 * n]
        send_sems, recv_sems, local_sems = refs[2 * n:2 * n + 3]
        local, _, waits = _split_copies(view, src_refs, land_refs, send_sems, recv_sems, local_sems)
        for cp in waits:
            cp.wait_send()
            cp.wait_recv()
        for cp in local:
            cp.wait()

    out = pl.pallas_call(
        body, name=name,
        out_shape=(*[pltpu.HBM(s.shape, s.dtype) for s in srcs], *[pltpu.HBM(l.shape, l.dtype) for l in lands]),
        in_specs=[HBM] * (2 * n) + [SEM, SEM, SEM] + [ANY] * len(afters),
        out_specs=tuple([HBM] * (2 * n)),
        input_output_aliases={i: i for i in range(2 * n)},
        compiler_params=pltpu.CompilerParams(has_side_effects=EFFECT),
    )(*srcs, *lands, *sems, *afters)
    handle["srcs"] = list(out[:n])
    return list(out[n:])


def _tie(x, token, name):
    def body(x_ref, t_ref, o_ref):
        pass

    return pl.pallas_call(
        body, name=name, out_shape=jax.ShapeDtypeStruct(x.shape, x.dtype),
        in_specs=[ANY, ANY], out_specs=ANY, input_output_aliases={0: 0},
    )(x, token)


def _mod_fwd(a, w, b):
    def body(a_ref, w_ref, b_ref, o_ref):
        o_ref[...] = _dot(_silu(a_ref[...]), w_ref[...], NN, precision=HI) + b_ref[...]

    return pl.pallas_call(
        body, name="mod_fwd", out_shape=jax.ShapeDtypeStruct((a.shape[0], w.shape[1]), F32),
        compiler_params=pltpu.CompilerParams(vmem_limit_bytes=VMEM_LIMIT),
    )(a, w, b)


def _mod_bwd(a, d, w):
    def body(a_ref, d_ref, w_ref, dw_ref, dc_ref):
        av = a_ref[...]
        dv = d_ref[...]
        dw_ref[...] = _dot(_silu(av), dv, TN, precision=HI)
        da = _dot(dv[0:8, :], w_ref[...], NT, precision=HI) * _dsilu(av[0:8, :])
        row = lax.broadcasted_iota(jnp.int32, da.shape, 0)
        dc_ref[...] = jnp.where(row == 0, da, 0.0)

    return pl.pallas_call(
        body, name="mod_bwd",
        out_shape=[jax.ShapeDtypeStruct(w.shape, F32), jax.ShapeDtypeStruct((8, w.shape[0]), F32)],
        compiler_params=pltpu.CompilerParams(vmem_limit_bytes=VMEM_LIMIT),
    )(a, d, w)


def _sum_devices(g):
    def body(g_ref, o_ref):
        acc = g_ref[0]
        for i in range(1, g.shape[0]):
            acc = acc + g_ref[i]
        o_ref[...] = acc

    return pl.pallas_call(body, name="sum_devices_%d" % g.shape[1],
                          out_shape=jax.ShapeDtypeStruct(g.shape[1:], F32))(g)


def _sum_windows(g, name):
    n, r, c = g.shape
    tr = 128 if r % 128 == 0 else 32

    def body(g_ref, o_ref):
        x, y, cc = _place()
        lane0 = (W_IN_SHARD * _slot((x, y, cc))) % DH
        acc = g_ref[0].astype(F32)
        for i in range(1, n):
            acc = acc + g_ref[i].astype(F32)
        o_ref[...] = pltpu.roll(acc, (c - lane0) % c, 1)

    return pl.pallas_call(
        body, name=name, grid=(r // tr,),
        in_specs=[pl.BlockSpec((n, tr, c), lambda i: (0, i, 0))],
        out_specs=pl.BlockSpec((tr, c), lambda i: (i, 0)),
        out_shape=jax.ShapeDtypeStruct((r, c), F32),
        compiler_params=_cp("parallel"),
    )(g)


def _adam_rows(r, c, n):
    budget = 6 * 1024 * 1024
    best = None
    for tr in range(16, r + 1, 16):
        if r % tr == 0 and tr * c * (2 * n + 28) <= budget:
            best = tr
    return best if best is not None else r


def _adamw(g, w, m, v, name):
    n, r, c = g.shape
    tr = _adam_rows(r, c, n)
    bc1 = 1.0 - ADAM_B1 ** ADAM_STEP
    bc2 = 1.0 - ADAM_B2 ** ADAM_STEP

    def body(g_ref, w_ref, m_ref, v_ref, go_ref, d_ref, mo_ref, vo_ref):
        grad = g_ref[0].astype(F32)
        for i in range(1, n):
            grad = grad + g_ref[i].astype(F32)
        go_ref[...] = grad
        m_new = ADAM_B1 * m_ref[...] + (1.0 - ADAM_B1) * grad
        v_new = ADAM_B2 * v_ref[...] + (1.0 - ADAM_B2) * (grad * grad)
        mo_ref[...] = m_new
        vo_ref[...] = v_new
        d_ref[...] = -ADAM_LR * ((m_new / bc1) / (jnp.sqrt(v_new / bc2) + ADAM_EPS) + ADAM_WD * w_ref[...])

    blk = pl.BlockSpec((tr, c), lambda i: (i, 0))
    out = jax.ShapeDtypeStruct((r, c), F32)
    return pl.pallas_call(
        body, name=name, grid=(r // tr,),
        in_specs=[pl.BlockSpec((n, tr, c), lambda i: (0, i, 0)), blk, blk, blk],
        out_specs=[blk] * 4, out_shape=[out] * 4,
        compiler_params=_cp("parallel"),
    )(g, w, m, v)


def kernel(x, c, ctx, c_ctx, w_mod, b_mod, norm_pre1, norm_post1, norm_pre2, norm_post2, w_in, hg_lb, hg_onorm, gla_w_gk, gla_b_gk, gla_onorm, w_br_hg, w_br_gla, w_out, w_ff_gate, w_ff_up, w_ff_down, loss_target, m_c_ctx, m_w_mod, m_b_mod, m_norm_pre1, m_norm_post1, m_norm_pre2, m_norm_post2, m_w_in, m_hg_lb, m_hg_onorm, m_gla_w_gk, m_gla_b_gk, m_gla_onorm, m_w_br_hg, m_w_br_gla, m_w_out, m_w_ff_gate, m_w_ff_up, m_w_ff_down, v_c_ctx, v_w_mod, v_b_mod, v_norm_pre1, v_norm_post1, v_norm_pre2, v_norm_post2, v_w_in, v_hg_lb, v_hg_onorm, v_gla_w_gk, v_gla_b_gk, v_gla_onorm, v_w_br_hg, v_w_br_gla, v_w_out, v_w_ff_gate, v_w_ff_up, v_w_ff_down):
    xi, yi, ci = lax.axis_index("x"), lax.axis_index("y"), lax.axis_index("c")
    me = 4 * xi + 2 * yi + ci
    t = CTX + x.shape[1]

    c_all, lb_g, wgk_g, bgk_g = _all_gather([c, hg_lb, gla_w_gk[0], gla_b_gk[0]], "ag_small")
    tr_ = lambda a: jnp.swapaxes(a[0], 0, 1)
    big = [w_in[0], w_br_hg[0], w_br_gla[0], w_out[0], tr_(w_ff_gate), tr_(w_ff_up), w_ff_down[0]]
    big_bf = [w.astype(BF16) for w in big]
    g_in, = _all_gather(big_bf[:1], "ag_w_in")
    gathered = lambda arrs: [(N_DEV,) + a.shape for a in arrs]
    mix_handle, tok = _split_start(_view_whole, gathered(big_bf[1:4]), big_bf[1:4], "ag_mix_start", g_in)
    ffn_handle, tok = _split_start(_view_whole, gathered(big_bf[4:]), big_bf[4:], "ag_ffn_start", tok)
    cols = lambda g: jnp.transpose(g, (1, 0, 2)).reshape(g.shape[1], N_DEV * g.shape[2])
    w_in_k = _tie(_layout_w_in(cols(g_in)), tok, "tie_w_in")

    def get_mix(after):
        g_brh, g_brg, g_out = _split_wait(mix_handle, "ag_mix_wait", after)
        return _gate_cols(cols(g_brh)), _gate_cols(cols(g_brg)), _gate_rows(g_out.reshape(D, D))

    def get_ffn(after):
        g_gate, g_up, g_down = _split_wait(ffn_handle, "ag_ffn_wait", after)
        return jnp.concatenate([g_gate.reshape(D_FF, D), g_up.reshape(D_FF, D)], axis=0), g_down.reshape(D_FF, D)

    hg_lb_full = jnp.transpose(lb_g, (1, 2, 0, 3)).reshape(2, 2, HW)
    wgk_k = _layout_wgk(jnp.transpose(wgk_g, (1, 2, 0, 3)).reshape(2, 16, HW)).astype(BF16)
    bgk_k = jnp.transpose(bgk_g, (1, 0, 2)).reshape(1, D)
    onw = jnp.concatenate([jnp.tile(hg_onorm, (1, NH // 2)), jnp.tile(gla_onorm, (1, NH // 2))], axis=1)

    n_mod = w_mod.shape[2]
    a9 = jnp.concatenate([c_ctx[None], c_all[:, 0], jnp.zeros((16 - 1 - N_DEV, D), F32)], axis=0)
    b_loc = lax.dynamic_slice(b_mod, (0, me * n_mod), (1, n_mod))
    s_loc = _mod_fwd(a9, w_mod[0], b_loc)
    s_all, = _all_gather([s_loc], "ag_mod")
    mod_all = jnp.transpose(s_all, (1, 0, 2)).reshape(16, N_DEV * n_mod)
    pad8 = lambda m: jnp.concatenate([m.reshape(6, D), jnp.zeros((2, D), F32)], axis=0)
    modc = pad8(mod_all[0])
    modx = pad8(lax.dynamic_slice(mod_all, (1 + me, 0), (1, N_DEV * n_mod))[0])

    z = jnp.concatenate([ctx[0], x[0]], axis=0)
    norms = (norm_pre1, norm_post1, norm_pre2, norm_post2)
    shard = lambda d: jnp.transpose(d.reshape(d.shape[0], N_DEV, -1), (1, 0, 2)).astype(BF16)
    rowshard = lambda d: d.reshape(N_DEV, d.shape[0] // N_DEV, d.shape[1]).astype(BF16)
    sent, w_in_grad = [], {}

    def send_w_in(i, x_after):
        half, rows = W_IN_GRAD_CHUNKS[i]
        handle, tok = _split_start(_view_window(rows), [(N_DEV, rows[1] - rows[0], D)], w_in_grad[half],
                                   "grads_w_in%d_start" % i, x_after)
        w_in_grad[half] = handle["srcs"]
        sent.append(("w_in%d" % i, ["w_in#%d" % i], handle))
        return _tie(x_after, tok, "tie_w_in%d" % i)

    def send(names, grads, x_after):
        if names == ("w_in_a",):
            w_in_grad["a"] = list(grads)
            return send_w_in(0, x_after)
        if names == ("w_in_b",):
            w_in_grad["b"] = list(grads)
            return x_after
        arrs, leaves = [], []
        for nm, g in zip(names, grads):
            if nm == "w_gu_t":
                arrs += [rowshard(g[:D_FF]), rowshard(g[D_FF:])]
                leaves += ["w_ff_gate", "w_ff_up"]
            elif nm == "w_down":
                arrs.append(rowshard(g))
                leaves.append("w_ff_down")
            elif nm == "w_out":
                arrs.append(rowshard(g[GOFF:GOFF + D]))
                leaves.append(nm)
            else:
                arrs.append(shard(g[:, GOFF:GOFF + D]))
                leaves.append(nm)
        handle, tok = _split_start(_view_block, [a.shape for a in arrs], arrs, "grads_%s_start" % names[0], x_after)
        sent.append((names[0], leaves, handle))
        return _tie(x_after, tok, "tie_" + names[0])

    r = _local_step(z, loss_target[0], modc, modx, norms, onw, hg_lb_full, wgk_k, bgk_k,
                    w_in_k, get_mix, get_ffn, send)
    grad_x = r["grad_x"][None]

    sm_pre, sm_mid, sm_fin = r["sm_pre"], r["sm_mid"], r["sm_final"]
    dmodc = jnp.stack([sm_pre[0], sm_pre[2], sm_mid[4], sm_mid[0], sm_mid[2], sm_fin[0]]).reshape(-1)
    dmodx = jnp.stack([sm_pre[1], sm_pre[3], sm_mid[5], sm_mid[1], sm_mid[3], sm_fin[1]]).reshape(-1)
    on = r["sm_post"][0].reshape(NH, DH)
    pieces = [dmodc, dmodx, sm_pre[4], sm_mid[7], sm_mid[6], sm_fin[2], on[:NH // 2].sum(0), on[NH // 2:].sum(0),
              r["d_lb"][:2].reshape(-1), _unlayout_wgk(r["d_wgk"]).reshape(-1), r["d_bgk"][0]]
    loss_local = (0.5 / D) * jnp.sum(r["loss_vec"])
    pieces.append(jnp.concatenate([loss_local.reshape(1), jnp.zeros((DH - 1,), F32)]))
    sizes = [p.shape[0] for p in pieces]
    pack = jnp.concatenate(pieces).reshape(-1, DH)
    pack_all, = _all_gather([pack], "ag_small_grads")
    pack_all = send_w_in(1, pack_all)
    tot = _sum_devices(pack_all).reshape(-1)
    offs = [sum(sizes[:i]) for i in range(len(sizes))]
    part = lambda i: tot[offs[i]:offs[i] + sizes[i]]
    dmodc_t, dmodx_t = part(0), part(1)
    g_b_mod = (dmodc_t + dmodx_t)[None]
    g_norms = [part(i)[None] for i in (2, 3, 4, 5)]
    g_hg_on, g_gla_on = part(6)[None], part(7)[None]
    lb0 = lax.dynamic_slice(part(8).reshape(2, HW), (0, me * (HW // N_DEV)), (2, HW // N_DEV))
    g_hg_lb = jnp.stack([lb0, -lb0])
    g_wgk = lax.dynamic_slice(part(9).reshape(2, 16, HW), (0, 0, me * (HW // N_DEV)), (2, 16, HW // N_DEV))[None]
    g_bgk = lax.dynamic_slice(part(10).reshape(2, HW), (0, me * (HW // N_DEV)), (2, HW // N_DEV))[None]
    loss = part(11)[0]

    dmx_all = pack_all.reshape(N_DEV, -1)[:, sizes[0]:sizes[0] + sizes[1]]
    d9 = jnp.concatenate([lax.dynamic_slice(dmodc_t[None], (0, me * n_mod), (1, n_mod)),
                          lax.dynamic_slice(dmx_all, (0, me * n_mod), (N_DEV, n_mod)),
                          jnp.zeros((16 - 1 - N_DEV, n_mod), F32)], axis=0)
    g_w_mod, dcc_part = _mod_bwd(a9, d9, w_mod[0])
    dcc_all, = _all_gather([dcc_part], "ag_c_ctx")
    dcc_all = send_w_in(2, dcc_all)
    g_c_ctx = _sum_devices(dcc_all)[0]

    recv = {}
    for first, leaves, handle in sent:
        if not first.startswith("w_in"):
            recv.update(zip(leaves, _split_wait(handle, "grads_%s_wait" % first, g_c_ctx)))
    moms = [(m_w_in, v_w_in), (m_w_br_hg, v_w_br_hg), (m_w_br_gla, v_w_br_gla), (m_w_out, v_w_out),
            (m_w_ff_gate, v_w_ff_gate), (m_w_ff_up, v_w_ff_up), (m_w_ff_down, v_w_ff_down)]
    names = ["w_in", "w_br_hg", "w_br_gla", "w_out", "w_ff_gate", "w_ff_up", "w_ff_down"]
    res = {}

    def update(nm, w, m, v):
        if nm in ("w_ff_gate", "w_ff_up"):
            outs = _adamw(recv[nm], w, tr_(m), tr_(v), "adamw_" + nm)
            res[nm] = [jnp.swapaxes(o, 0, 1)[None] for o in outs]
        else:
            res[nm] = [o[None] for o in _adamw(recv[nm], w, m[0], v[0], "adamw_" + nm)]

    for nm, w, (m, v) in list(zip(names, big, moms))[1:]:
        update(nm, w, m, v)
    res["w_mod"] = [o[None] for o in _adamw(g_w_mod[None], w_mod[0], m_w_mod[0], v_w_mod[0], "adamw_w_mod")]

    small = [("c_ctx", c_ctx, m_c_ctx, v_c_ctx, g_c_ctx), ("b_mod", b_mod, m_b_mod, v_b_mod, g_b_mod),
             ("norm_pre1", norm_pre1, m_norm_pre1, v_norm_pre1, g_norms[0]),
             ("norm_post1", norm_post1, m_norm_post1, v_norm_post1, g_norms[1]),
             ("norm_pre2", norm_pre2, m_norm_pre2, v_norm_pre2, g_norms[2]),
             ("norm_post2", norm_post2, m_norm_post2, v_norm_post2, g_norms[3]),
             ("hg_lb", hg_lb, m_hg_lb, v_hg_lb, g_hg_lb), ("hg_onorm", hg_onorm, m_hg_onorm, v_hg_onorm, g_hg_on),
             ("gla_w_gk", gla_w_gk, m_gla_w_gk, v_gla_w_gk, g_wgk), ("gla_b_gk", gla_b_gk, m_gla_b_gk, v_gla_b_gk, g_bgk),
             ("gla_onorm", gla_onorm, m_gla_onorm, v_gla_onorm, g_gla_on)]
    flat = lambda k: jnp.concatenate([s[k].reshape(-1) for s in small]).reshape(-1, DH)
    outs = _adamw(flat(4)[None], flat(1), flat(2), flat(3), "adamw_small")
    off = 0
    for nm, w, _, _, _ in small:
        res[nm] = [o.reshape(-1)[off:off + w.size].reshape(w.shape) for o in outs]
        off += w.size

    done = [res[nm][0] for nm in names[1:]] + [res["w_mod"][0], outs[0]]
    sums = []
    for i, (first, leaves, handle) in enumerate(s for s in sent if s[0].startswith("w_in")):
        half = W_IN_GRAD_CHUNKS[i][0]
        land, = _split_wait(handle, "grads_%s_wait" % first, done, srcs=w_in_grad[half])
        w_in_grad[half] = handle["srcs"]
        sums.append(_sum_windows(land, "sum_windows%d" % i))
    recv["w_in"] = jnp.concatenate(sums, axis=0)[None, :, :W_IN_SHARD]
    update("w_in", big[0], *moms[0])

    order = ["c_ctx", "w_mod", "b_mod", "norm_pre1", "norm_post1", "norm_pre2", "norm_post2", "w_in", "hg_lb",
             "hg_onorm", "gla_w_gk", "gla_b_gk", "gla_onorm", "w_br_hg", "w_br_gla", "w_out", "w_ff_gate", "w_ff_up",
             "w_ff_down"]
    return (loss, grad_x, *[res[n][k] for k in range(4) for n in order])
```

```python
import functools

import jax
import jax.numpy as jnp
from jax import lax
from jax.experimental import pallas as pl
from jax.experimental.pallas import tpu as pltpu

F32 = jnp.float32
BF16 = jnp.bfloat16
HI = lax.Precision.HIGHEST

N_DEV = 8
D = 1024
CTX = 256
HW = 512
DH = 128
NH = 8
D_FF = 2816
EPS = 1e-6
GLA_NORM = 16.0
CHUNK = 64
TR = 256
NCT = CTX // TR
W_IN_COLS = 7168
MAIN0 = 0
LR0 = 4608
GW = 1152
GOFF = 32
GATE_HG0 = LR0
GATE_GLA0 = LR0 + D
LEVELS = (32, 16, 8)
EXP_CLAMP = 80.0
VMEM_LIMIT = 48 * 1024 * 1024

ADAM_LR, ADAM_B1, ADAM_B2, ADAM_EPS, ADAM_WD, ADAM_STEP = 0.001, 0.9, 0.999, 1e-08, 0.01, 10


def _cp(*sem):
    return pltpu.CompilerParams(dimension_semantics=sem, vmem_limit_bytes=VMEM_LIMIT)


def _sig(x):
    return jax.nn.sigmoid(x)


def _silu(x):
    return x * _sig(x)


def _dsilu(x):
    s = _sig(x)
    return s * (1.0 + x * (1.0 - s))


def _rstd(x):
    return lax.rsqrt(jnp.mean(x * x, axis=-1, keepdims=True) + EPS)


def _rms_bwd(a, y, r):
    return r * (a - y * (r * r) * jnp.mean(a * y, axis=-1, keepdims=True))


def _colsum(x):
    return jnp.sum(x, axis=0, keepdims=True)


def _dot(a, b, dims, precision=None):
    return lax.dot_general(a, b, (dims, ((), ())), preferred_element_type=F32, precision=precision)


NN = ((1,), (0,))
NT = ((1,), (1,))
TN = ((0,), (0,))

SCAN_HEADS_FWD = 4
SCAN_HEADS_BWD = 4


def _split_dot(m, x):
    mb = m.astype(BF16)
    x1 = x.astype(BF16)
    r1 = x - x1.astype(F32)
    x2 = r1.astype(BF16)
    x3 = (r1 - x2.astype(F32)).astype(BF16)
    return _dot(mb, x1, NN) + _dot(mb, x2, NN) + _dot(mb, x3, NN)


def _matmul(a, b, dims, out_dtype, name, tm, tn, tk, a_off=0, m_out=None):
    pair = isinstance(b, (tuple, list))
    bs = list(b) if pair else [b]
    b1 = bs[0]
    rows = b1.shape[0] * len(bs)
    half = None
    if dims == NN:
        m, k, n = a.shape[0], rows, b1.shape[1]
        a_spec = pl.BlockSpec((tm, tk), lambda i, j, kk: (i, kk + a_off))
        half = b1.shape[0] // tk
        b_maps = [lambda i, j, kk: (kk, j)] if not pair else [
            lambda i, j, kk: (jnp.minimum(kk, half - 1), j), lambda i, j, kk: (jnp.maximum(kk - half, 0), j)]
        b_specs = [pl.BlockSpec((tk, tn), f) for f in b_maps]
        axis = 2
    elif dims == NT:
        m, k, n = a.shape[0], b1.shape[1], rows
        a_spec = pl.BlockSpec((tm, tk), lambda i, j, kk: (i, kk + a_off))
        half = b1.shape[0] // tn
        b_maps = [lambda i, j, kk: (j, kk)] if not pair else [
            lambda i, j, kk: (jnp.minimum(j, half - 1), kk), lambda i, j, kk: (jnp.maximum(j - half, 0), kk)]
        b_specs = [pl.BlockSpec((tn, tk), f) for f in b_maps]
        axis = 1
    else:
        assert not pair
        m, k = (a.shape[1] if m_out is None else m_out), a.shape[0]
        n = b1.shape[1]
        a_spec = pl.BlockSpec((tk, tm), lambda i, j, kk: (kk, i + a_off))
        b_specs = [pl.BlockSpec((tk, tn), lambda i, j, kk: (kk, j))]
    assert m % tm == 0 and n % tn == 0 and k % tk == 0, (name, m, n, k, tm, tn, tk)
    nk = k // tk
    nb = len(bs)

    def body(a_ref, *refs):
        o_ref = refs[nb]
        if pair:
            bv = jnp.where(pl.program_id(axis) < half, refs[0][...], refs[1][...])
        else:
            bv = refs[0][...]
        part = _dot(a_ref[...], bv, dims)
        if nk == 1:
            o_ref[...] = part.astype(o_ref.dtype)
            return
        acc_ref = refs[nb + 1]
        kk = pl.program_id(2)

        @pl.when(kk == 0)
        def _():
            acc_ref[...] = part

        @pl.when(kk > 0)
        def _():
            acc_ref[...] += part

        @pl.when(kk == nk - 1)
        def _():
            o_ref[...] = acc_ref[...].astype(o_ref.dtype)

    return pl.pallas_call(
        body,
        name=name,
        grid=(m // tm, n // tn, nk),
        in_specs=[a_spec] + b_specs,
        out_specs=pl.BlockSpec((tm, tn), lambda i, j, kk: (i, j)),
        out_shape=jax.ShapeDtypeStruct((m, n), out_dtype),
        scratch_shapes=[] if nk == 1 else [pltpu.VMEM((tm, tn), F32)],
        compiler_params=_cp("parallel", "parallel", "arbitrary"),
    )(a, *bs)


def _row(c):
    return pl.BlockSpec((TR, c), lambda i: (i, 0))


def _rowcol(width, cb):
    return pl.BlockSpec((TR, width), lambda i: (i, cb))


def _full(shape):
    return pl.BlockSpec(shape, lambda i: (0,) * len(shape))


def _mod_row(mc_ref, mx_ref, k, is_ctx):
    return jnp.where(is_ctx, mc_ref[k:k + 1, :], mx_ref[k:k + 1, :])


def _acc_row(ref, k, val):
    ref[k:k + 1, :] += val


def _acc_mod(ref, k, is_ctx, val):
    zero = jnp.zeros_like(val)
    ref[k:k + 1, :] += jnp.where(is_ctx, val, zero)
    ref[k + 1:k + 2, :] += jnp.where(is_ctx, zero, val)


def _prenorm(z, nw, modc, modx, i_shift, i_scale, name):
    t = z.shape[0]

    def body(z_ref, nw_ref, mc_ref, mx_ref, h_ref):
        is_ctx = pl.program_id(0) < NCT
        x = z_ref[...]
        n = x * _rstd(x) * nw_ref[...]
        h = n * (1.0 + _mod_row(mc_ref, mx_ref, i_scale, is_ctx)) + _mod_row(mc_ref, mx_ref, i_shift, is_ctx)
        h_ref[...] = h.astype(BF16)

    return pl.pallas_call(
        body, name=name, grid=(t // TR,),
        in_specs=[_row(D), _full((1, D)), _full((8, D)), _full((8, D))],
        out_specs=_row(D),
        out_shape=jax.ShapeDtypeStruct((t, D), BF16),
        compiler_params=_cp("parallel"),
    )(z, nw, modc, modx)


def _hg_lb(lb_ref, d):
    a0 = lb_ref[0, d:d + 1, :]
    a1 = lb_ref[1, d:d + 1, :]
    mx = jnp.maximum(a0, a1)
    e0 = jnp.exp(a0 - mx)
    e1 = jnp.exp(a1 - mx)
    return e0 / (e0 + e1)


def _log_sigmoid(x):
    return jnp.minimum(x, 0.0) - jnp.log(1.0 + jnp.exp(-jnp.abs(x)))


def _gates_fwd(p, hg_lb, wgk, bgk):
    t = p.shape[0]
    seg = lambda j: _rowcol(HW, MAIN0 // HW + j)

    def body(hq_ref, hi_ref, hf_ref, hb_ref, gq_ref, gk_ref, gv_ref, lr_ref, lb_ref, wgk_ref, bgk_ref,
             q_ref, v_ref, kf_ref, kb_ref, gf_ref, gb_ref):
        q_ref[:, :HW] = _silu(hq_ref[...].astype(F32))
        q_ref[:, HW:] = gq_ref[...].astype(F32) * (DH ** -0.5)
        v_ref[:, :HW] = hi_ref[...].astype(F32)
        v_ref[:, HW:] = gv_ref[...].astype(F32)
        xg = _dot(lr_ref[...].astype(BF16), wgk_ref[...], NN) + bgk_ref[...]
        for d, (raw_ref, k_ref, g_ref) in enumerate(((hf_ref, kf_ref, gf_ref), (hb_ref, kb_ref, gb_ref))):
            lbd = _hg_lb(lb_ref, d)
            f = lbd + (1.0 - lbd) * _sig(raw_ref[...].astype(F32))
            k_ref[:, :HW] = 1.0 - f
            k_ref[:, HW:] = gk_ref[...].astype(F32)
            g_ref[:, :HW] = jnp.log(f)
            g_ref[:, HW:] = _log_sigmoid(xg[:, d * HW:(d + 1) * HW]) * (1.0 / GLA_NORM)

    out = jax.ShapeDtypeStruct((t, D), F32)
    return pl.pallas_call(
        body, name="gates_fwd", grid=(t // TR,),
        in_specs=[seg(0), seg(1), seg(2), seg(3), seg(5), seg(6), seg(7), _rowcol(DH, LR0 // DH),
                  _full((2, 2, HW)), _full((DH, D)), _full((1, D))],
        out_specs=[_row(D)] * 6,
        out_shape=[out] * 6,
        compiler_params=_cp("parallel"),
    )(p, p, p, p, p, p, p, p, hg_lb, wgk, bgk)


def _post_fwd(o_fw, o_bw, p, onw):
    t = o_fw.shape[0]

    def body(of_ref, ob_ref, g1_ref, g2_ref, w_ref, y_ref):
        for h in range(NH):
            sl = slice(h * DH, (h + 1) * DH)
            o = of_ref[:, sl] + ob_ref[:, sl]
            g_ref = g1_ref if h < NH // 2 else g2_ref
            gs = slice((h % (NH // 2)) * DH, (h % (NH // 2) + 1) * DH)
            n = o * _rstd(o) * w_ref[:, sl]
            y_ref[:, sl] = (n * _silu(g_ref[:, gs].astype(F32))).astype(BF16)

    return pl.pallas_call(
        body, name="post_fwd", grid=(t // TR,),
        in_specs=[_row(D), _row(D), _rowcol(HW, MAIN0 // HW + 4), _rowcol(HW, MAIN0 // HW + 8), _full((1, D))],
        out_specs=_row(D),
        out_shape=jax.ShapeDtypeStruct((t, D), BF16),
        compiler_params=_cp("parallel"),
    )(o_fw, o_bw, p, p, onw)


def _gate_window_specs(col0):
    return [_rowcol(HW, col0 // HW), _rowcol(HW, col0 // HW + 1), _rowcol(DH, (col0 + 2 * HW) // DH)]


def _gate_window(refs):
    return jnp.concatenate([r[...].astype(F32) for r in refs], axis=1)


def _merge_fwd(p, u1, u2):
    t = p.shape[0]

    def body(a0, a1, a2, b0, b1, b2, u1_ref, u2_ref, m_ref):
        f = lambda r: r[...].astype(F32)
        m_ref[...] = (_sig(_gate_window((a0, a1, a2))) * f(u1_ref)
                      + _sig(_gate_window((b0, b1, b2))) * f(u2_ref)).astype(BF16)

    return pl.pallas_call(
        body, name="merge_fwd", grid=(t // TR,),
        in_specs=_gate_window_specs(GATE_HG0) + _gate_window_specs(GATE_GLA0) + [_row(GW), _row(GW)],
        out_specs=_row(GW),
        out_shape=jax.ShapeDtypeStruct((t, GW), BF16),
        compiler_params=_cp("parallel"),
    )(p, p, p, p, p, p, u1, u2)


def _mid_fwd(z, y1, nw_post, nw_pre, modc, modx):
    t = z.shape[0]

    def body(z_ref, y_ref, wpo_ref, wpr_ref, mc_ref, mx_ref, z1_ref, h_ref):
        is_ctx = pl.program_id(0) < NCT
        y = y_ref[...]
        z1 = z_ref[...] + _mod_row(mc_ref, mx_ref, 2, is_ctx) * (y * _rstd(y) * wpo_ref[...])
        z1_ref[...] = z1
        n = z1 * _rstd(z1) * wpr_ref[...]
        h = n * (1.0 + _mod_row(mc_ref, mx_ref, 4, is_ctx)) + _mod_row(mc_ref, mx_ref, 3, is_ctx)
        h_ref[...] = h.astype(BF16)

    return pl.pallas_call(
        body, name="mid_fwd", grid=(t // TR,),
        in_specs=[_row(D), _row(D), _full((1, D)), _full((1, D)), _full((8, D)), _full((8, D))],
        out_specs=[_row(D), _row(D)],
        out_shape=[jax.ShapeDtypeStruct((t, D), F32), jax.ShapeDtypeStruct((t, D), BF16)],
        compiler_params=_cp("parallel"),
    )(z, y1, nw_post, nw_pre, modc, modx)


def _swiglu_fwd(uv):
    t = uv.shape[0]

    def body(u_ref, v_ref, a_ref):
        a_ref[...] = (_silu(u_ref[...].astype(F32)) * v_ref[...].astype(F32)).astype(BF16)

    return pl.pallas_call(
        body, name="swiglu_fwd", grid=(t // TR,),
        in_specs=[_rowcol(D_FF, 0), _rowcol(D_FF, 1)],
        out_specs=_row(D_FF),
        out_shape=jax.ShapeDtypeStruct((t, D_FF), BF16),
        compiler_params=_cp("parallel"),
    )(uv, uv)


def _swiglu_bwd(uv, da):
    t = uv.shape[0]

    def body(u_ref, v_ref, da_ref, d_ref):
        u = u_ref[...].astype(F32)
        d = da_ref[...].astype(F32)
        d_ref[:, :D_FF] = (d * v_ref[...].astype(F32) * _dsilu(u)).astype(BF16)
        d_ref[:, D_FF:] = (d * _silu(u)).astype(BF16)

    return pl.pallas_call(
        body, name="swiglu_bwd", grid=(t // TR,),
        in_specs=[_rowcol(D_FF, 0), _rowcol(D_FF, 1), _row(D_FF)],
        out_specs=_row(2 * D_FF),
        out_shape=jax.ShapeDtypeStruct((t, 2 * D_FF), BF16),
        compiler_params=_cp("parallel"),
    )(uv, uv, da)


def _final(z1, y2, target, nw, modc, modx):
    t = z1.shape[0]

    def body(z1_ref, y_ref, tg_ref, w_ref, mc_ref, mx_ref, dz_ref, dy_ref, loss_ref, sm_ref):
        i = pl.program_id(0)
        is_ctx = i < NCT

        @pl.when(i == 0)
        def _():
            loss_ref[...] = jnp.zeros_like(loss_ref)
            sm_ref[...] = jnp.zeros_like(sm_ref)

        g = _mod_row(mc_ref, mx_ref, 5, is_ctx)
        y = y_ref[...]
        r = _rstd(y)
        w = w_ref[...]
        yr = y * r
        n = yr * w
        e = z1_ref[...] + g * n - tg_ref[...]
        lat = jnp.where(is_ctx, 0.0, 1.0)
        loss_ref[...] += lat * _colsum(e * e)
        dz = e * (lat / D)
        dz_ref[...] = dz
        _acc_mod(sm_ref, 0, is_ctx, _colsum(dz * n))
        dn = dz * g
        _acc_row(sm_ref, 2, _colsum(dn * yr))
        dy_ref[...] = _rms_bwd(dn * w, y, r).astype(BF16)

    return pl.pallas_call(
        body, name="final", grid=(t // TR,),
        in_specs=[_row(D), _row(D), pl.BlockSpec((TR, D), lambda i: (jnp.maximum(i - NCT, 0), 0)),
                  _full((1, D)), _full((8, D)), _full((8, D))],
        out_specs=[_row(D), _row(D), _full((1, D)), _full((8, D))],
        out_shape=[jax.ShapeDtypeStruct((t, D), F32), jax.ShapeDtypeStruct((t, D), BF16),
                   jax.ShapeDtypeStruct((1, D), F32), jax.ShapeDtypeStruct((8, D), F32)],
        compiler_params=_cp("arbitrary"),
    )(z1, y2, target, nw, modc, modx)


def _mid_bwd(dh2, dz, z, z1, y1, nw_post, nw_pre, modc, modx):
    t = z.shape[0]

    def body(dh_ref, dz_ref, z_ref, z1_ref, y_ref, wpo_ref, wpr_ref, mc_ref, mx_ref, dzo_ref, dy_ref, sm_ref):
        i = pl.program_id(0)
        is_ctx = i < NCT

        @pl.when(i == 0)
        def _():
            sm_ref[...] = jnp.zeros_like(sm_ref)

        dh = dh_ref[...]
        z1 = z1_ref[...]
        r = _rstd(z1)
        zr = z1 * r
        wpr = wpr_ref[...]
        n = zr * wpr
        _acc_mod(sm_ref, 0, is_ctx, _colsum(dh))
        _acc_mod(sm_ref, 2, is_ctx, _colsum(dh * n))
        dn = dh * (1.0 + _mod_row(mc_ref, mx_ref, 4, is_ctx))
        _acc_row(sm_ref, 6, _colsum(dn * zr))
        dz1 = dz_ref[...] + _rms_bwd(dn * wpr, z1, r)
        dzo_ref[...] = dz1
        y = y_ref[...]
        r1 = _rstd(y)
        yr = y * r1
        wpo = wpo_ref[...]
        g = _mod_row(mc_ref, mx_ref, 2, is_ctx)
        _acc_mod(sm_ref, 4, is_ctx, _colsum(dz1 * (yr * wpo)))
        dn1 = dz1 * g
        _acc_row(sm_ref, 7, _colsum(dn1 * yr))
        dy_ref[...] = _rms_bwd(dn1 * wpo, y, r1).astype(BF16)

    return pl.pallas_call(
        body, name="mid_bwd", grid=(t // TR,),
        in_specs=[_row(D)] * 5 + [_full((1, D)), _full((1, D)), _full((8, D)), _full((8, D))],
        out_specs=[_row(D), _row(D), _full((8, D))],
        out_shape=[jax.ShapeDtypeStruct((t, D), F32), jax.ShapeDtypeStruct((t, D), BF16),
                   jax.ShapeDtypeStruct((8, D), F32)],
        compiler_params=_cp("arbitrary"),
    )(dh2, dz, z, z1, y1, nw_post, nw_pre, modc, modx)


def _pre_bwd(dh1, dz, z, nw, modc, modx):
    t = z.shape[0]

    def body(dh_ref, dz_ref, z_ref, w_ref, mc_ref, mx_ref, dzo_ref, sm_ref):
        i = pl.program_id(0)
        is_ctx = i < NCT

        @pl.when(i == 0)
        def _():
            sm_ref[...] = jnp.zeros_like(sm_ref)

        dh = dh_ref[...]
        x = z_ref[...]
        r = _rstd(x)
        xr = x * r
        w = w_ref[...]
        _acc_mod(sm_ref, 0, is_ctx, _colsum(dh))
        _acc_mod(sm_ref, 2, is_ctx, _colsum(dh * (xr * w)))
        dn = dh * (1.0 + _mod_row(mc_ref, mx_ref, 1, is_ctx))
        _acc_row(sm_ref, 4, _colsum(dn * xr))
        dzo_ref[...] = dz_ref[...] + _rms_bwd(dn * w, x, r)

    return pl.pallas_call(
        body, name="pre_bwd", grid=(t // TR,),
        in_specs=[_row(D)] * 3 + [_full((1, D)), _full((8, D)), _full((8, D))],
        out_specs=[pl.BlockSpec((TR, D), lambda i: (jnp.maximum(i - NCT, 0), 0)), _full((8, D))],
        out_shape=[jax.ShapeDtypeStruct((t - CTX, D), F32), jax.ShapeDtypeStruct((8, D), F32)],
        compiler_params=_cp("arbitrary"),
    )(dh1, dz, z, nw, modc, modx)


def _merge_bwd(dm, p, u1, u2):
    t = dm.shape[0]

    def body(dm_ref, a0, a1, a2, b0, b1, b2, u1_ref, u2_ref, du1_ref, du2_ref, dg_ref):
        dm_ = dm_ref[...]
        s1 = _sig(_gate_window((a0, a1, a2)))
        s2 = _sig(_gate_window((b0, b1, b2)))
        du1_ref[...] = (dm_ * s1).astype(BF16)
        du2_ref[...] = (dm_ * s2).astype(BF16)
        dg_ref[:, :GW] = (dm_ * u1_ref[...].astype(F32) * s1 * (1.0 - s1)).astype(BF16)
        dg_ref[:, GW:] = (dm_ * u2_ref[...].astype(F32) * s2 * (1.0 - s2)).astype(BF16)

    return pl.pallas_call(
        body, name="merge_bwd", grid=(t // TR,),
        in_specs=[_row(GW)] + _gate_window_specs(GATE_HG0) + _gate_window_specs(GATE_GLA0) + [_row(GW), _row(GW)],
        out_specs=[_row(GW), _row(GW), _row(2 * GW)],
        out_shape=[jax.ShapeDtypeStruct((t, GW), BF16), jax.ShapeDtypeStruct((t, GW), BF16),
                   jax.ShapeDtypeStruct((t, 2 * GW), BF16)],
        compiler_params=_cp("parallel"),
    )(dm, p, p, p, p, p, p, u1, u2)


def _post_bwd(dy_hg, dy_gla, o_fw, o_bw, p, onw):
    t = o_fw.shape[0]

    def body(d1_ref, d2_ref, of_ref, ob_ref, g1_ref, g2_ref, w_ref, do_ref, dg_ref, sm_ref):
        @pl.when(pl.program_id(0) == 0)
        def _():
            sm_ref[...] = jnp.zeros_like(sm_ref)

        for h in range(NH):
            sl = slice(h * DH, (h + 1) * DH)
            gs = slice((h % (NH // 2)) * DH, (h % (NH // 2) + 1) * DH)
            g_ref, d_ref = (g1_ref, d1_ref) if h < NH // 2 else (g2_ref, d2_ref)
            o = of_ref[:, sl] + ob_ref[:, sl]
            r = _rstd(o)
            orr = o * r
            w = w_ref[:, sl]
            gt = g_ref[:, gs].astype(F32)
            dy = d_ref[:, gs]
            dg_ref[:, sl] = (dy * (orr * w) * _dsilu(gt)).astype(BF16)
            dn = dy * _silu(gt)
            sm_ref[0:1, sl] += _colsum(dn * orr)
            do_ref[:, sl] = _rms_bwd(dn * w, o, r)

    return pl.pallas_call(
        body, name="post_bwd", grid=(t // TR,),
        in_specs=[_row(HW), _row(HW), _row(D), _row(D), _rowcol(HW, MAIN0 // HW + 4), _rowcol(HW, MAIN0 // HW + 8),
                  _full((1, D))],
        out_specs=[_row(D), _row(D), _full((8, D))],
        out_shape=[jax.ShapeDtypeStruct((t, D), F32), jax.ShapeDtypeStruct((t, D), BF16),
                   jax.ShapeDtypeStruct((8, D), F32)],
        compiler_params=_cp("arbitrary"),
    )(dy_hg, dy_gla, o_fw, o_bw, p, p, onw)


def _gates_bwd(p, hg_lb, wgk, bgk, dgm, dgo, dq_f, dq_b, dv_f, dv_b, dk_f, dk_b, dg_f, dg_b):
    t = p.shape[0]
    seg = lambda j: _rowcol(HW, MAIN0 // HW + j)

    def body(hq_ref, hf_ref, hb_ref, lr_ref, lb_ref, wgk_ref, bgk_ref, dgm_ref, dgo_ref,
             dqf_ref, dqb_ref, dvf_ref, dvb_ref, dkf_ref, dkb_ref, dgf_ref, dgb_ref,
             dp_ref, dlb_ref, dw_ref, db_ref):
        @pl.when(pl.program_id(0) == 0)
        def _():
            dlb_ref[...] = jnp.zeros_like(dlb_ref)
            dw_ref[...] = jnp.zeros_like(dw_ref)
            db_ref[...] = jnp.zeros_like(db_ref)

        c0 = MAIN0

        def put(j, val):
            dp_ref[:, c0 + j * HW:c0 + (j + 1) * HW] = val.astype(BF16)

        dq = dqf_ref[...] + dqb_ref[...]
        dv = dvf_ref[...] + dvb_ref[...]
        put(0, dq[:, :HW] * _dsilu(hq_ref[...].astype(F32)))
        put(1, dv[:, :HW])
        put(5, dq[:, HW:] * (DH ** -0.5))
        put(7, dv[:, HW:])
        put(6, dkf_ref[:, HW:] + dkb_ref[:, HW:])
        dp_ref[:, c0 + 4 * HW:c0 + 5 * HW] = dgo_ref[:, :HW]
        dp_ref[:, c0 + 8 * HW:c0 + 9 * HW] = dgo_ref[:, HW:]
        lr = lr_ref[...].astype(BF16)
        xg = _dot(lr, wgk_ref[...], NN) + bgk_ref[...]
        dxg = []
        for d, (raw_ref, dk_ref, dg_ref) in enumerate(((hf_ref, dkf_ref, dgf_ref), (hb_ref, dkb_ref, dgb_ref))):
            lbd = _hg_lb(lb_ref, d)
            s = _sig(raw_ref[...].astype(F32))
            f = lbd + (1.0 - lbd) * s
            df = dg_ref[:, :HW] / f - dk_ref[:, :HW]
            put(2 + d, df * (1.0 - lbd) * s * (1.0 - s))
            dlb_ref[d:d + 1, :] += _colsum(df * (1.0 - s)) * (lbd * (1.0 - lbd))
            dxg.append(dg_ref[:, HW:] * (1.0 / GLA_NORM) * _sig(-xg[:, d * HW:(d + 1) * HW]))
        dxg = jnp.concatenate(dxg, axis=1)
        db_ref[0:1, :] += _colsum(dxg)
        dxg_b = dxg.astype(BF16)
        dw_ref[...] += _dot(lr, dxg_b, TN)
        dlr = _dot(dxg_b, wgk_ref[...], NT)
        dp_ref[:, LR0:LR0 + DH] = (dlr + dgm_ref[:, :DH].astype(F32)).astype(BF16)
        dp_ref[:, LR0 + DH:GATE_GLA0] = dgm_ref[:, DH:D]
        dp_ref[:, GATE_GLA0:GATE_GLA0 + DH] = dgm_ref[:, D:GW] + dgm_ref[:, GW:GW + DH]
        dp_ref[:, GATE_GLA0 + DH:GATE_GLA0 + GW] = dgm_ref[:, GW + DH:]
        dp_ref[:, GATE_GLA0 + GW:] = jnp.zeros((TR, W_IN_COLS - GATE_GLA0 - GW), BF16)

    return pl.pallas_call(
        body, name="gates_bwd", grid=(t // TR,),
        in_specs=[seg(0), seg(2), seg(3), _rowcol(DH, LR0 // DH), _full((2, 2, HW)), _full((DH, D)), _full((1, D)),
                  _row(2 * GW), _row(D)] + [_row(D)] * 8,
        out_specs=[_row(W_IN_COLS), _full((8, HW)), _full((DH, D)), _full((8, D))],
        out_shape=[jax.ShapeDtypeStruct((t, W_IN_COLS), BF16), jax.ShapeDtypeStruct((8, HW), F32),
                   jax.ShapeDtypeStruct((DH, D), F32), jax.ShapeDtypeStruct((8, D), F32)],
        compiler_params=_cp("arbitrary"),
    )(p, p, p, p, hg_lb, wgk, bgk, dgm, dgo, dq_f, dq_b, dv_f, dv_b, dk_f, dk_b, dg_f, dg_b)


def _scan_consts(rev):
    r = lax.broadcasted_iota(jnp.int32, (CHUNK, CHUNK), 0)
    u = lax.broadcasted_iota(jnp.int32, (CHUNK, CHUNK), 1)
    rp = lax.broadcasted_iota(jnp.int32, (CHUNK, 1), 0)
    if rev:
        r, u, rp = CHUNK - 1 - r, CHUNK - 1 - u, CHUNK - 1 - rp
    tri = jnp.where(u <= r, 1.0, 0.0).astype(F32)
    tri_t = jnp.where(r <= u, 1.0, 0.0).astype(F32)
    lv = []
    for b in LEVELS:
        sh = b.bit_length() - 1
        pair = ((r >> sh) == (u >> sh) + 1) & (((u >> sh) & 1) == 0)
        pair_t = ((u >> sh) == (r >> sh) + 1) & (((r >> sh) & 1) == 0)
        tside = ((rp >> sh) & 1) == 1
        lv.append((pair, pair_t, tside))
    bd = LEVELS[-1].bit_length() - 1
    diag = ((r >> bd) == (u >> bd)) & (u <= r)
    diag_t = ((r >> bd) == (u >> bd)) & (r <= u)
    return tri, tri_t, lv, diag, diag_t


def _row_of(pos, rev):
    return CHUNK - 1 - pos if rev else pos


def _chunk_terms(cum, b_scr, consts, rev):
    _, _, lv, _, _ = consts
    terms = []
    for b, (_, _, tside) in zip(LEVELS, lv):
        pieces = []
        for j in range(CHUNK // (2 * b)):
            row = _row_of(2 * b * j + b - 1, rev)
            pieces.append(jnp.broadcast_to(b_scr[row:row + 1, :], (2 * b, DH)))
        if rev:
            pieces = pieces[::-1]
        bnd = pieces[0] if len(pieces) == 1 else jnp.concatenate(pieces, axis=0)
        w = jnp.exp(jnp.minimum(jnp.where(tside, cum - bnd, bnd - cum), 0.0))
        wq = jnp.where(tside, w, 0.0)
        wk = jnp.where(tside, 0.0, w)
        terms.append((wq, wk))
    b = LEVELS[-1]
    pieces = []
    for j in range(CHUNK // b):
        if j == 0:
            pieces.append(jnp.zeros((b, DH), F32))
        else:
            row = _row_of(b * j - 1, rev)
            pieces.append(jnp.broadcast_to(b_scr[row:row + 1, :], (b, DH)))
    if rev:
        pieces = pieces[::-1]
    start = jnp.concatenate(pieces, axis=0)
    wq = jnp.exp(jnp.minimum(cum - start, 0.0))
    wk = jnp.exp(jnp.minimum(start - cum, EXP_CLAMP))
    terms.append((wq, wk))
    return terms


def _run_staged(units):
    live = list(units)
    while live:
        nxt = []
        for u in live:
            try:
                next(u)
                nxt.append(u)
            except StopIteration:
                pass
        live = nxt


SCAN_TB = 256
SCAN_CB = SCAN_TB // CHUNK


def _block_order(i, ntb, rev):
    nctx = CTX // SCAN_TB
    if not rev:
        return i
    return jnp.where(i < nctx, nctx - 1 - i, ntb - 1 - (i - nctx))


def _chunk_in_block(j, rev):
    return SCAN_CB - 1 - j if rev else j


def _scan_fwd(q, k, v, g, rev):
    t = q.shape[0]
    nc = t // CHUNK
    hpb = SCAN_HEADS_FWD

    def body(q_ref, k_ref, v_ref, g_ref, o_ref, st_ref, s_scr, b_scr):
        consts = _scan_consts(rev)
        _, _, lv, diag, _ = consts
        masks = [pair for pair, _, _ in lv] + [diag]

        @pl.when(pl.program_id(1) == 0)
        def _():
            s_scr[...] = jnp.zeros_like(s_scr)

        tri = consts[0]
        state = {hh: s_scr[hh] for hh in range(hpb)}

        def unit(hh, j):
            sl = slice(hh * DH, (hh + 1) * DH)
            c = _chunk_in_block(j, rev)
            rows = slice(c * CHUNK, (c + 1) * CHUNK)
            b_ref = b_scr.at[hh * SCAN_CB + j]
            qc, kc, vc, gc = q_ref[rows, sl], k_ref[rows, sl], v_ref[rows, sl], g_ref[rows, sl]
            cum = _split_dot(tri, gc)
            b_ref[...] = cum
            yield
            terms = _chunk_terms(cum, b_ref, consts, rev)
            ops = [((qc * wq).astype(BF16), (kc * wk).astype(BF16)) for wq, wk in terms]
            tot = _colsum(gc)
            qe = (qc * jnp.exp(cum)).astype(BF16)
            ke = (kc * jnp.exp(tot - cum)).astype(BF16)
            vb = vc.astype(BF16)
            yield
            scs = [_dot(qt, kt, NT) for qt, kt in ops]
            kv = _dot(vb, ke, TN)
            yield
            a = jnp.zeros((CHUNK, CHUNK), F32)
            for sc, m in zip(scs, masks):
                a = a + jnp.where(m, sc, 0.0)
            o_intra = _dot(a.astype(BF16), vb, NN)
            yield
            st = state[hh]
            st_ref[hh, c] = st
            o_ref[rows, sl] = o_intra + _dot(qe, st.astype(BF16), NT)
            state[hh] = st * jnp.exp(tot) + kv
            yield

        _run_staged([unit(hh, j) for hh in range(hpb) for j in range(SCAN_CB)])
        for hh in range(hpb):
            s_scr[hh] = state[hh]

    ntb = t // SCAN_TB
    col = pl.BlockSpec((SCAN_TB, hpb * DH), lambda h, i: (_block_order(i, ntb, rev), h))
    return pl.pallas_call(
        body, name="scan_fwd_" + ("bw" if rev else "fw"), grid=(NH // hpb, ntb),
        in_specs=[col] * 4,
        out_specs=[col, pl.BlockSpec((hpb, SCAN_CB, DH, DH), lambda h, i: (h, _block_order(i, ntb, rev), 0, 0))],
        out_shape=[jax.ShapeDtypeStruct((t, D), F32), jax.ShapeDtypeStruct((NH, nc, DH, DH), F32)],
        scratch_shapes=[pltpu.VMEM((hpb, DH, DH), F32), pltpu.VMEM((hpb * SCAN_CB, CHUNK, DH), F32)],
        compiler_params=_cp("parallel", "arbitrary"),
    )(q, k, v, g)


def _scan_bwd(q, k, v, g, do, states, rev):
    t = q.shape[0]
    nc = t // CHUNK
    hpb = SCAN_HEADS_BWD

    def body(q_ref, k_ref, v_ref, g_ref, do_ref, st_ref, dq_ref, dk_ref, dv_ref, dg_ref, ds_scr, b_scr):
        consts = _scan_consts(rev)
        _, tri_t, lv, diag, diag_t = consts
        masks = [(pair, pair_t) for pair, pair_t, _ in lv] + [(diag, diag_t)]
        @pl.when(pl.program_id(1) == 0)
        def _():
            ds_scr[...] = jnp.zeros_like(ds_scr)

        tri = consts[0]
        dstate = {hh: ds_scr[hh] for hh in range(hpb)}

        def unit(hh, jj):
            sl = slice(hh * DH, (hh + 1) * DH)
            c = _chunk_in_block(SCAN_CB - 1 - jj, rev)
            rows = slice(c * CHUNK, (c + 1) * CHUNK)
            b_ref = b_scr.at[hh * SCAN_CB + jj]
            qc, kc, vc, gc = q_ref[rows, sl], k_ref[rows, sl], v_ref[rows, sl], g_ref[rows, sl]
            dob = do_ref[rows, sl].astype(BF16)
            vb = vc.astype(BF16)
            cum = _split_dot(tri, gc)
            b_ref[...] = cum
            da = _dot(dob, vb, NT)
            da_t = _dot(vb, dob, NT)
            yield
            terms = _chunk_terms(cum, b_ref, consts, rev)
            ops = [((qc * wq).astype(BF16), (kc * wk).astype(BF16)) for wq, wk in terms]
            tot = _colsum(gc)
            e_tot = jnp.exp(tot)
            e_b = jnp.exp(cum)
            e_t = jnp.exp(tot - cum)
            qeb = (qc * e_b).astype(BF16)
            keb = (kc * e_t).astype(BF16)
            dal = [(jnp.where(m, da, 0.0).astype(BF16), jnp.where(m_t, da_t, 0.0).astype(BF16)) for m, m_t in masks]
            yield
            ats = [_dot(ktb, qtb, NT) for qtb, ktb in ops]
            dqts = [_dot(d, ktb, NN) for (d, _), (_, ktb) in zip(dal, ops)]
            dkts = [_dot(d_t, qtb, NN) for (_, d_t), (qtb, _) in zip(dal, ops)]
            qd = _dot(dob, qeb, TN)
            yield
            a_t = jnp.zeros((CHUNK, CHUNK), F32)
            dq = jnp.zeros((CHUNK, DH), F32)
            dk = jnp.zeros((CHUNK, DH), F32)
            db = jnp.zeros((CHUNK, DH), F32)
            for at, dqt, dkt, (wq, wk), (qtb, ktb), (_, m_t) in zip(ats, dqts, dkts, terms, ops, masks):
                a_t = a_t + jnp.where(m_t, at, 0.0)
                dq = dq + dqt * wq
                dk = dk + dkt * wk
                db = db + dqt * qtb.astype(F32) - dkt * ktb.astype(F32)
            dv_intra = _dot(a_t.astype(BF16), dob, NN)
            st = st_ref[hh, c]
            stb = st.astype(BF16)
            dqe = _dot(dob, stb, NN)
            yield
            dst = dstate[hh]
            dstb = dst.astype(BF16)
            dstate[hh] = dst * e_tot + qd
            dv_ref[rows, sl] = dv_intra + _dot(keb, dstb, NT)
            dke = _dot(vb, dstb, NN)
            yield
            qe = qeb.astype(F32)
            ke = keb.astype(F32)
            dq_ref[rows, sl] = dq + dqe * e_b
            dk_ref[rows, sl] = dk + dke * e_t
            db = db + dqe * qe - dke * ke
            dtot = _colsum(dstb.astype(F32) * stb.astype(F32)) * e_tot + _colsum(dke * ke)
            dg_ref[rows, sl] = _split_dot(tri_t, db) + dtot
            yield

        _run_staged([unit(hh, jj) for hh in range(hpb) for jj in range(SCAN_CB)])
        for hh in range(hpb):
            ds_scr[hh] = dstate[hh]

    ntb = t // SCAN_TB
    blk = lambda i: _block_order(ntb - 1 - i, ntb, rev)
    col = pl.BlockSpec((SCAN_TB, hpb * DH), lambda h, i: (blk(i), h))
    out = jax.ShapeDtypeStruct((t, D), F32)
    return pl.pallas_call(
        body, name="scan_bwd_" + ("bw" if rev else "fw"), grid=(NH // hpb, ntb),
        in_specs=[col] * 5 + [pl.BlockSpec((hpb, SCAN_CB, DH, DH), lambda h, i: (h, blk(i), 0, 0))],
        out_specs=[col] * 4,
        out_shape=[out] * 4,
        scratch_shapes=[pltpu.VMEM((hpb, DH, DH), F32), pltpu.VMEM((hpb * SCAN_CB, CHUNK, DH), F32)],
        compiler_params=_cp("parallel", "arbitrary"),
    )(q, k, v, g, do, states)


W_IN_GRAD_CHUNKS = (("a", (0, 512)), ("b", (0, 96)), ("b", (96, 512)))
W_IN_REF = 6688


def _layout_w_in(w):
    return jnp.pad(w, ((0, 0), (0, W_IN_COLS - W_IN_REF)))


def _unlayout_w_in(d):
    return d[:, :W_IN_REF]


def _gate_cols(w):
    return jnp.pad(w, ((0, 0), (GOFF, GW - GOFF - D)))


def _gate_rows(w):
    return jnp.pad(w, ((GOFF, GW - GOFF - D), (0, 0)))


def _layout_wgk(w):
    r = w.shape[1]
    top = jnp.concatenate([w[0], jnp.zeros_like(w[0])], axis=1)
    bot = jnp.concatenate([jnp.zeros_like(w[1]), w[1]], axis=1)
    return jnp.concatenate([top, bot, jnp.zeros((DH - 2 * r, D), w.dtype)], axis=0)


def _unlayout_wgk(d, r=16):
    return jnp.stack([d[:r, :HW], d[r:2 * r, HW:]])


def _local_step(z, target, modc, modx, norms, onw, hg_lb, wgk, bgk, w_in, get_mix, get_ffn, send):
    n_pre1, n_post1, n_pre2, n_post2 = norms
    t = z.shape[0]
    tm = 768 if t % 768 == 0 else 256
    h1 = _prenorm(z, n_pre1, modc, modx, 0, 1, "prenorm1")
    p = _matmul(h1, w_in, NN, BF16, "mm_in", tm, 512, D)
    q, v, k_f, k_b, g_f, g_b = _gates_fwd(p, hg_lb, wgk, bgk)
    o_f, st_f = _scan_fwd(q, k_f, v, g_f, False)
    o_b, st_b = _scan_fwd(q, k_b, v, g_b, True)
    y = _post_fwd(o_f, o_b, p, onw)
    w_br_hg, w_br_gla, w_out = get_mix(y)
    u1 = _matmul(y, w_br_hg, NN, BF16, "mm_br_hg", tm, GW, HW, a_off=0)
    u2 = _matmul(y, w_br_gla, NN, BF16, "mm_br_gla", tm, GW, HW, a_off=1)
    merged = _merge_fwd(p, u1, u2)
    y1 = _matmul(merged, w_out, NN, F32, "mm_out", tm, 512, GW)
    z1, h2 = _mid_fwd(z, y1, n_post1, n_pre2, modc, modx)
    w_gu_t, w_down = get_ffn(h2)
    uv = _matmul(h2, w_gu_t, NT, BF16, "mm_gu", tm, D_FF // 2, D)
    act = _swiglu_fwd(uv)
    y2 = _matmul(act, w_down, NN, F32, "mm_down", tm, 512, D_FF // 2)
    dz, dy2, loss_vec, sm_final = _final(z1, y2, target, n_post2, modc, modx)
    dact = _matmul(dy2, w_down, NT, BF16, "mm_down_dx", tm, D_FF // 2, D)
    d_w_down = _matmul(act, dy2, TN, BF16, "mm_down_dw", D_FF // 2, 512, t)
    duv = _swiglu_bwd(uv, dact)
    dh2 = _matmul(duv, w_gu_t, NN, F32, "mm_gu_dx", tm, 512, D_FF // 2)
    d_w_gate_t = _matmul(duv, h2, TN, BF16, "mm_gate_dw", D_FF // 2, 512, t, a_off=0, m_out=D_FF)
    d_w_up_t = _matmul(duv, h2, TN, BF16, "mm_up_dw", D_FF // 2, 512, t, a_off=2, m_out=D_FF)
    dh2 = send(("w_down", "w_gate_t", "w_up_t"), (d_w_down, d_w_gate_t, d_w_up_t), dh2)
    dz, dy1, sm_mid = _mid_bwd(dh2, dz, z, z1, y1, n_post1, n_pre2, modc, modx)
    dmerged = _matmul(dy1, w_out, NT, F32, "mm_out_dx", tm, GW, D)
    d_w_out = _matmul(merged, dy1, TN, BF16, "mm_out_dw", GW, 512, t)
    du1, du2, dgm = _merge_bwd(dmerged, p, u1, u2)
    dy_hg = _matmul(du1, w_br_hg, NT, F32, "mm_br_hg_dx", tm, HW, GW)
    dy_gla = _matmul(du2, w_br_gla, NT, F32, "mm_br_gla_dx", tm, HW, GW)
    d_w_br_hg = _matmul(y, du1, TN, BF16, "mm_br_hg_dw", HW, GW, t, a_off=0, m_out=HW)
    d_w_br_gla = _matmul(y, du2, TN, BF16, "mm_br_gla_dw", HW, GW, t, a_off=1, m_out=HW)
    dy_hg = send(("w_out", "w_br_hg", "w_br_gla"), (d_w_out, d_w_br_hg, d_w_br_gla), dy_hg)
    do, dgo, sm_post = _post_bwd(dy_hg, dy_gla, o_f, o_b, p, onw)
    dq_f, dk_f, dv_f, dg_f = _scan_bwd(q, k_f, v, g_f, do, st_f, False)
    dq_b, dk_b, dv_b, dg_b = _scan_bwd(q, k_b, v, g_b, do, st_b, True)
    dp, d_lb, d_wgk, d_bgk = _gates_bwd(p, hg_lb, wgk, bgk, dgm, dgo, dq_f, dq_b, dv_f, dv_b, dk_f, dk_b, dg_f, dg_b)
    d_w_in_a = _matmul(h1, dp, TN, BF16, "mm_in_dw_a", 512, 512, t, a_off=0, m_out=D // 2)
    dp = send(("w_in_a",), (d_w_in_a,), dp)
    d_w_in_b = _matmul(h1, dp, TN, BF16, "mm_in_dw_b", 512, 512, t, a_off=1, m_out=D // 2)
    dp = send(("w_in_b",), (d_w_in_b,), dp)
    dh1 = _matmul(dp, w_in, NT, F32, "mm_in_dx", tm, 512, 1024)
    grad_x, sm_pre = _pre_bwd(dh1, dz, z, n_pre1, modc, modx)
    return dict(loss_vec=loss_vec, grad_x=grad_x, sm_final=sm_final, sm_mid=sm_mid, sm_post=sm_post, sm_pre=sm_pre,
                d_lb=d_lb, d_wgk=d_wgk, d_bgk=d_bgk)


MESH = pl.DeviceIdType.MESH
ANY = pl.BlockSpec(memory_space=pl.ANY)
N_REL = N_DEV - 1


def _place():
    return lax.axis_index("x"), lax.axis_index("y"), lax.axis_index("c")


def _slot(p):
    return 4 * p[0] + 2 * p[1] + p[2]


def _all_gather(arrays, name):
    n = len(arrays)

    def body(*refs):
        ins, outs = refs[:n], refs[n:2 * n]
        send_sems, recv_sems, local_sems = refs[2 * n:]
        x, y, c = _place()
        me, sibling = (x, y, c), (x, y, 1 - c)
        chips = [(1 - x, y), (x, 1 - y), (1 - x, 1 - y)]

        def copy(a, k, block, to, src=None):
            dst = outs[a].at[_slot(block)]
            return pltpu.make_async_remote_copy(
                src_ref=dst if src is None else src, dst_ref=dst,
                send_sem=send_sems.at[N_REL * a + k], recv_sem=recv_sems.at[N_REL * a + k],
                device_id=to, device_id_type=MESH)

        mine = [pltpu.make_async_copy(ins[a], outs[a].at[_slot(me)], local_sems.at[a]) for a in range(n)]
        for cp in mine:
            cp.start()
        first = []
        for a in range(n):
            first.append(copy(a, 0, me, sibling, src=ins[a]))
            first += [copy(a, 1 + j, me, (*chip, c), src=ins[a]) for j, chip in enumerate(chips)]
        for cp in first:
            cp.start()
        passed = []
        for j, chip in enumerate(chips):
            for a in range(n):
                copy(a, 1 + j, (*chip, c), me).wait_recv()
                fwd = copy(a, 4 + j, (*chip, c), sibling)
                fwd.start()
                passed.append(fwd)
        for a in range(n):
            copy(a, 0, sibling, me).wait_recv()
        for j, chip in enumerate(chips):
            for a in range(n):
                copy(a, 4 + j, (*chip, 1 - c), me).wait_recv()
        for cp in first + passed:
            cp.wait_send()
        for cp in mine:
            cp.wait()

    return pl.pallas_call(
        body, name=name,
        in_specs=[ANY] * n, out_specs=[ANY] * n,
        out_shape=[jax.ShapeDtypeStruct((N_DEV,) + a.shape, a.dtype) for a in arrays],
        scratch_shapes=[pltpu.SemaphoreType.DMA((N_REL * n,)), pltpu.SemaphoreType.DMA((N_REL * n,)),
                        pltpu.SemaphoreType.DMA((n,))],
    )(*arrays)


def _exchange(arrays, name):
    n = len(arrays)

    def body(*refs):
        ins, outs = refs[:n], refs[n:2 * n]
        send_sems, recv_sems, local_sems = refs[2 * n:]
        x, y, c = _place()
        me = _slot((x, y, c))
        mine = [pltpu.make_async_copy(ins[a].at[me], outs[a].at[me], local_sems.at[a]) for a in range(n)]
        for cp in mine:
            cp.start()
        copies = []
        for a in range(n):
            for k in range(1, N_DEV):
                flip = lambda v, bit: 1 - v if bit else v
                peer = (flip(x, k & 4), flip(y, k & 2), flip(c, k & 1))
                copies.append(pltpu.make_async_remote_copy(
                    src_ref=ins[a].at[_slot(peer)], dst_ref=outs[a].at[me],
                    send_sem=send_sems.at[N_REL * a + k - 1], recv_sem=recv_sems.at[N_REL * a + k - 1],
                    device_id=peer, device_id_type=MESH))
                copies[-1].start()
        i = 0
        for a in range(n):
            for k in range(1, N_DEV):
                flip = lambda v, bit: 1 - v if bit else v
                peer = (flip(x, k & 4), flip(y, k & 2), flip(c, k & 1))
                pltpu.make_async_remote_copy(
                    src_ref=ins[a].at[_slot(peer)], dst_ref=outs[a].at[_slot(peer)],
                    send_sem=send_sems.at[N_REL * a + k - 1], recv_sem=recv_sems.at[N_REL * a + k - 1],
                    device_id=peer, device_id_type=MESH).wait_recv()
                i += 1
        for cp in copies:
            cp.wait_send()
        for cp in mine:
            cp.wait()

    return pl.pallas_call(
        body, name=name,
        in_specs=[ANY] * n, out_specs=[ANY] * n,
        out_shape=[jax.ShapeDtypeStruct(a.shape, a.dtype) for a in arrays],
        scratch_shapes=[pltpu.SemaphoreType.DMA((N_REL * n,)), pltpu.SemaphoreType.DMA((N_REL * n,)),
                        pltpu.SemaphoreType.DMA((n,))],
    )(*arrays)


HBM = pl.BlockSpec(memory_space=pltpu.HBM)
SEM = pl.BlockSpec(memory_space=pltpu.SEMAPHORE)
EFFECT = pltpu.SideEffectType.DATAFLOW_SIDE_EFFECTING


def _peer_of(x, y, c, k):
    flip = lambda v, bit: 1 - v if bit else v
    return flip(x, k & 4), flip(y, k & 2), flip(c, k & 1)


def _view_whole(src, slot):
    return src


def _view_block(src, slot):
    return src.at[slot]


W_IN_SHARD = W_IN_REF // N_DEV


def _view_window(rows):
    def view(src, slot):
        col0 = pl.multiple_of((W_IN_SHARD * slot // DH) * DH, DH)
        return src.at[pl.ds(rows[0], rows[1] - rows[0]), pl.ds(col0, D)]
    return view


def _split_copies(view, srcs, lands, send_sems, recv_sems, local_sems):
    x, y, c = _place()
    me = _slot((x, y, c))
    local, sends, waits = [], [], []
    for a, (src, land) in enumerate(zip(srcs, lands)):
        local.append(pltpu.make_async_copy(view(src, me), land.at[me], local_sems.at[a]))
        for k in range(1, N_DEV):
            peer = _peer_of(x, y, c, k)
            mine = view(src, _slot(peer))
            sems = dict(send_sem=send_sems.at[N_REL * a + k - 1], recv_sem=recv_sems.at[N_REL * a + k - 1],
                        device_id=peer, device_id_type=MESH)
            sends.append(pltpu.make_async_remote_copy(src_ref=mine, dst_ref=land.at[me], **sems))
            waits.append(pltpu.make_async_remote_copy(src_ref=mine, dst_ref=land.at[_slot(peer)], **sems))
    return local, sends, waits


def _split_start(view, land_shapes, srcs, name, after):
    n = len(srcs)
    lands = [lax.empty(shp, s.dtype) for shp, s in zip(land_shapes, srcs)]

    def body(*refs):
        src_refs, land_refs = refs[:n], refs[n:2 * n]
        send_sems, recv_sems, local_sems = refs[2 * n + 1:2 * n + 4]
        token = refs[-1]
        local, sends, _ = _split_copies(view, src_refs, land_refs, send_sems, recv_sems, local_sems)
        for cp in local + sends:
            cp.start()
        token[...] = jnp.zeros_like(token)

    hbm = lambda a: pltpu.with_memory_space_constraint(a, pltpu.HBM)
    out = pl.pallas_call(
        body, name=name,
        out_shape=(pltpu.SemaphoreType.DMA((N_REL * n,)), pltpu.SemaphoreType.DMA((N_REL * n,)),
                   pltpu.SemaphoreType.DMA((n,)),
                   *[pltpu.HBM(s.shape, s.dtype) for s in srcs], *[pltpu.HBM(l.shape, l.dtype) for l in lands],
                   jax.ShapeDtypeStruct((8, DH), F32)),
        in_specs=[HBM] * (2 * n) + [ANY],
        out_specs=(SEM, SEM, SEM, *([HBM] * (2 * n)), pl.BlockSpec(memory_space=pltpu.VMEM)),
        input_output_aliases={i: 3 + i for i in range(2 * n)},
        compiler_params=pltpu.CompilerParams(has_side_effects=EFFECT),
    )(*[hbm(s) for s in srcs], *[hbm(l) for l in lands], after)
    handle = dict(view=view, n=n, sems=out[:3], srcs=list(out[3:3 + n]), lands=list(out[3 + n:3 + 2 * n]))
    return handle, out[-1]


def _split_wait(handle, name, after, srcs=None):
    view, n, sems, lands = handle["view"], handle["n"], handle["sems"], handle["lands"]
    srcs = handle["srcs"] if srcs is None else srcs
    afters = list(after) if isinstance(after, (list, tuple)) else [after]

    def body(*refs):
        src_refs, land_refs = refs[:n], refs[n:2 * n]
        send_sems, recv_sems, local_sems = refs[2 * n:2 * n + 3]
        local, _, waits = _split_copies(view, src_refs, land_refs, send_sems, recv_sems, local_sems)
        for cp in waits:
            cp.wait_send()
            cp.wait_recv()
        for cp in local:
            cp.wait()

    out = pl.pallas_call(
        body, name=name,
        out_shape=(*[pltpu.HBM(s.shape, s.dtype) for s in srcs], *[pltpu.HBM(l.shape, l.dtype) for l in lands]),
        in_specs=[HBM] * (2 * n) + [SEM, SEM, SEM] + [ANY] * len(afters),
        out_specs=tuple([HBM] * (2 * n)),
        input_output_aliases={i: i for i in range(2 * n)},
        compiler_params=pltpu.CompilerParams(has_side_effects=EFFECT),
    )(*srcs, *lands, *sems, *afters)
    handle["srcs"] = list(out[:n])
    return list(out[n:])


def _tie(x, token, name):
    def body(x_ref, t_ref, o_ref):
        pass

    return pl.pallas_call(
        body, name=name, out_shape=jax.ShapeDtypeStruct(x.shape, x.dtype),
        in_specs=[ANY, ANY], out_specs=ANY, input_output_aliases={0: 0},
    )(x, token)


def _mod_fwd(a, w, b):
    def body(a_ref, w_ref, b_ref, o_ref):
        o_ref[...] = _dot(_silu(a_ref[...]), w_ref[...], NN, precision=HI) + b_ref[...]

    return pl.pallas_call(
        body, name="mod_fwd", out_shape=jax.ShapeDtypeStruct((a.shape[0], w.shape[1]), F32),
        compiler_params=pltpu.CompilerParams(vmem_limit_bytes=VMEM_LIMIT),
    )(a, w, b)


def _mod_bwd(a, d, w):
    def body(a_ref, d_ref, w_ref, dw_ref, dc_ref):
        av = a_ref[...]
        dv = d_ref[...]
        dw_ref[...] = _dot(_silu(av), dv, TN, precision=HI)
        da = _dot(dv[0:8, :], w_ref[...], NT, precision=HI) * _dsilu(av[0:8, :])
        row = lax.broadcasted_iota(jnp.int32, da.shape, 0)
        dc_ref[...] = jnp.where(row == 0, da, 0.0)

    return pl.pallas_call(
        body, name="mod_bwd",
        out_shape=[jax.ShapeDtypeStruct(w.shape, F32), jax.ShapeDtypeStruct((8, w.shape[0]), F32)],
        compiler_params=pltpu.CompilerParams(vmem_limit_bytes=VMEM_LIMIT),
    )(a, d, w)


def _sum_devices(g):
    def body(g_ref, o_ref):
        acc = g_ref[0]
        for i in range(1, g.shape[0]):
            acc = acc + g_ref[i]
        o_ref[...] = acc

    return pl.pallas_call(body, name="sum_devices_%d" % g.shape[1],
                          out_shape=jax.ShapeDtypeStruct(g.shape[1:], F32))(g)


def _sum_windows(g, name):
    n, r, c = g.shape
    tr = 128 if r % 128 == 0 else 32

    def body(g_ref, o_ref):
        x, y, cc = _place()
        lane0 = (W_IN_SHARD * _slot((x, y, cc))) % DH
        acc = g_ref[0].astype(F32)
        for i in range(1, n):
            acc = acc + g_ref[i].astype(F32)
        o_ref[...] = pltpu.roll(acc, (c - lane0) % c, 1)

    return pl.pallas_call(
        body, name=name, grid=(r // tr,),
        in_specs=[pl.BlockSpec((n, tr, c), lambda i: (0, i, 0))],
        out_specs=pl.BlockSpec((tr, c), lambda i: (i, 0)),
        out_shape=jax.ShapeDtypeStruct((r, c), F32),
        compiler_params=_cp("parallel"),
    )(g)


def _adam_rows(r, c, n):
    budget = 6 * 1024 * 1024
    best = None
    for tr in range(16, r + 1, 16):
        if r % tr == 0 and tr * c * (2 * n + 28) <= budget:
            best = tr
    return best if best is not None else r


def _adamw(g, w, m, v, name):
    n, r, c = g.shape
    tr = _adam_rows(r, c, n)
    bc1 = 1.0 - ADAM_B1 ** ADAM_STEP
    bc2 = 1.0 - ADAM_B2 ** ADAM_STEP

    def body(g_ref, w_ref, m_ref, v_ref, go_ref, d_ref, mo_ref, vo_ref):
        grad = g_ref[0].astype(F32)
        for i in range(1, n):
            grad = grad + g_ref[i].astype(F32)
        go_ref[...] = grad
        m_new = ADAM_B1 * m_ref[...] + (1.0 - ADAM_B1) * grad
        v_new = ADAM_B2 * v_ref[...] + (1.0 - ADAM_B2) * (grad * grad)
        mo_ref[...] = m_new
        vo_ref[...] = v_new
        d_ref[...] = -ADAM_LR * ((m_new / bc1) / (jnp.sqrt(v_new / bc2) + ADAM_EPS) + ADAM_WD * w_ref[...])

    blk = pl.BlockSpec((tr, c), lambda i: (i, 0))
    out = jax.ShapeDtypeStruct((r, c), F32)
    return pl.pallas_call(
        body, name=name, grid=(r // tr,),
        in_specs=[pl.BlockSpec((n, tr, c), lambda i: (0, i, 0)), blk, blk, blk],
        out_specs=[blk] * 4, out_shape=[out] * 4,
        compiler_params=_cp("parallel"),
    )(g, w, m, v)


def kernel(x, c, ctx, c_ctx, w_mod, b_mod, norm_pre1, norm_post1, norm_pre2, norm_post2, w_in, hg_lb, hg_onorm, gla_w_gk, gla_b_gk, gla_onorm, w_br_hg, w_br_gla, w_out, w_ff_gate, w_ff_up, w_ff_down, loss_target, m_c_ctx, m_w_mod, m_b_mod, m_norm_pre1, m_norm_post1, m_norm_pre2, m_norm_post2, m_w_in, m_hg_lb, m_hg_onorm, m_gla_w_gk, m_gla_b_gk, m_gla_onorm, m_w_br_hg, m_w_br_gla, m_w_out, m_w_ff_gate, m_w_ff_up, m_w_ff_down, v_c_ctx, v_w_mod, v_b_mod, v_norm_pre1, v_norm_post1, v_norm_pre2, v_norm_post2, v_w_in, v_hg_lb, v_hg_onorm, v_gla_w_gk, v_gla_b_gk, v_gla_onorm, v_w_br_hg, v_w_br_gla, v_w_out, v_w_ff_gate, v_w_ff_up, v_w_ff_down):
    xi, yi, ci = lax.axis_index("x"), lax.axis_index("y"), lax.axis_index("c")
    me = 4 * xi + 2 * yi + ci
    t = CTX + x.shape[1]

    c_all, lb_g, wgk_g, bgk_g = _all_gather([c, hg_lb, gla_w_gk[0], gla_b_gk[0]], "ag_small")
    tr_ = lambda a: jnp.swapaxes(a[0], 0, 1)
    big = [w_in[0], w_br_hg[0], w_br_gla[0], w_out[0], tr_(w_ff_gate), tr_(w_ff_up), w_ff_down[0]]
    big_bf = [w.astype(BF16) for w in big]
    g_in, = _all_gather(big_bf[:1], "ag_w_in")
    gathered = lambda arrs: [(N_DEV,) + a.shape for a in arrs]
    mix_handle, tok = _split_start(_view_whole, gathered(big_bf[1:4]), big_bf[1:4], "ag_mix_start", g_in)
    ffn_handle, tok = _split_start(_view_whole, gathered(big_bf[4:]), big_bf[4:], "ag_ffn_start", tok)
    cols = lambda g: jnp.transpose(g, (1, 0, 2)).reshape(g.shape[1], N_DEV * g.shape[2])
    w_in_k = _tie(_layout_w_in(cols(g_in)), tok, "tie_w_in")

    def get_mix(after):
        g_brh, g_brg, g_out = _split_wait(mix_handle, "ag_mix_wait", after)
        return _gate_cols(cols(g_brh)), _gate_cols(cols(g_brg)), _gate_rows(g_out.reshape(D, D))

    def get_ffn(after):
        g_gate, g_up, g_down = _split_wait(ffn_handle, "ag_ffn_wait", after)
        return (g_gate.reshape(D_FF, D), g_up.reshape(D_FF, D)), g_down.reshape(D_FF, D)

    hg_lb_full = jnp.transpose(lb_g, (1, 2, 0, 3)).reshape(2, 2, HW)
    wgk_k = _layout_wgk(jnp.transpose(wgk_g, (1, 2, 0, 3)).reshape(2, 16, HW)).astype(BF16)
    bgk_k = jnp.transpose(bgk_g, (1, 0, 2)).reshape(1, D)
    onw = jnp.concatenate([jnp.tile(hg_onorm, (1, NH // 2)), jnp.tile(gla_onorm, (1, NH // 2))], axis=1)

    n_mod = w_mod.shape[2]
    a9 = jnp.concatenate([c_ctx[None], c_all[:, 0], jnp.zeros((16 - 1 - N_DEV, D), F32)], axis=0)
    b_loc = lax.dynamic_slice(b_mod, (0, me * n_mod), (1, n_mod))
    s_loc = _mod_fwd(a9, w_mod[0], b_loc)
    s_all, = _all_gather([s_loc], "ag_mod")
    mod_all = jnp.transpose(s_all, (1, 0, 2)).reshape(16, N_DEV * n_mod)
    pad8 = lambda m: jnp.concatenate([m.reshape(6, D), jnp.zeros((2, D), F32)], axis=0)
    modc = pad8(mod_all[0])
    modx = pad8(lax.dynamic_slice(mod_all, (1 + me, 0), (1, N_DEV * n_mod))[0])

    z = jnp.concatenate([ctx[0], x[0]], axis=0)
    norms = (norm_pre1, norm_post1, norm_pre2, norm_post2)
    shard = lambda d: jnp.transpose(d.reshape(d.shape[0], N_DEV, -1), (1, 0, 2)).astype(BF16)
    rowshard = lambda d: d.reshape(N_DEV, d.shape[0] // N_DEV, d.shape[1]).astype(BF16)
    sent, w_in_grad = [], {}

    def send_w_in(i, x_after):
        half, rows = W_IN_GRAD_CHUNKS[i]
        handle, tok = _split_start(_view_window(rows), [(N_DEV, rows[1] - rows[0], D)], w_in_grad[half],
                                   "grads_w_in%d_start" % i, x_after)
        w_in_grad[half] = handle["srcs"]
        sent.append(("w_in%d" % i, ["w_in#%d" % i], handle))
        return _tie(x_after, tok, "tie_w_in%d" % i)

    def send(names, grads, x_after):
        if names == ("w_in_a",):
            w_in_grad["a"] = list(grads)
            return send_w_in(0, x_after)
        if names == ("w_in_b",):
            w_in_grad["b"] = list(grads)
            return x_after
        arrs, leaves = [], []
        for nm, g in zip(names, grads):
            if nm in ("w_gate_t", "w_up_t"):
                arrs.append(rowshard(g))
                leaves.append({"w_gate_t": "w_ff_gate", "w_up_t": "w_ff_up"}[nm])
            elif nm == "w_down":
                arrs.append(rowshard(g))
                leaves.append("w_ff_down")
            elif nm == "w_out":
                arrs.append(rowshard(g[GOFF:GOFF + D]))
                leaves.append(nm)
            else:
                arrs.append(shard(g[:, GOFF:GOFF + D]))
                leaves.append(nm)
        handle, tok = _split_start(_view_block, [a.shape for a in arrs], arrs, "grads_%s_start" % names[0], x_after)
        sent.append((names[0], leaves, handle))
        return _tie(x_after, tok, "tie_" + names[0])

    r = _local_step(z, loss_target[0], modc, modx, norms, onw, hg_lb_full, wgk_k, bgk_k,
                    w_in_k, get_mix, get_ffn, send)
    grad_x = r["grad_x"][None]

    sm_pre, sm_mid, sm_fin = r["sm_pre"], r["sm_mid"], r["sm_final"]
    dmodc = jnp.stack([sm_pre[0], sm_pre[2], sm_mid[4], sm_mid[0], sm_mid[2], sm_fin[0]]).reshape(-1)
    dmodx = jnp.stack([sm_pre[1], sm_pre[3], sm_mid[5], sm_mid[1], sm_mid[3], sm_fin[1]]).reshape(-1)
    on = r["sm_post"][0].reshape(NH, DH)
    pieces = [dmodc, dmodx, sm_pre[4], sm_mid[7], sm_mid[6], sm_fin[2], on[:NH // 2].sum(0), on[NH // 2:].sum(0),
              r["d_lb"][:2].reshape(-1), _unlayout_wgk(r["d_wgk"]).reshape(-1), r["d_bgk"][0]]
    loss_local = (0.5 / D) * jnp.sum(r["loss_vec"])
    pieces.append(jnp.concatenate([loss_local.reshape(1), jnp.zeros((DH - 1,), F32)]))
    sizes = [p.shape[0] for p in pieces]
    pack = jnp.concatenate(pieces).reshape(-1, DH)
    pack_all, = _all_gather([pack], "ag_small_grads")
    pack_all = send_w_in(1, pack_all)
    tot = _sum_devices(pack_all).reshape(-1)
    offs = [sum(sizes[:i]) for i in range(len(sizes))]
    part = lambda i: tot[offs[i]:offs[i] + sizes[i]]
    dmodc_t, dmodx_t = part(0), part(1)
    g_b_mod = (dmodc_t + dmodx_t)[None]
    g_norms = [part(i)[None] for i in (2, 3, 4, 5)]
    g_hg_on, g_gla_on = part(6)[None], part(7)[None]
    lb0 = lax.dynamic_slice(part(8).reshape(2, HW), (0, me * (HW // N_DEV)), (2, HW // N_DEV))
    g_hg_lb = jnp.stack([lb0, -lb0])
    g_wgk = lax.dynamic_slice(part(9).reshape(2, 16, HW), (0, 0, me * (HW // N_DEV)), (2, 16, HW // N_DEV))[None]
    g_bgk = lax.dynamic_slice(part(10).reshape(2, HW), (0, me * (HW // N_DEV)), (2, HW // N_DEV))[None]
    loss = part(11)[0]

    dmx_all = pack_all.reshape(N_DEV, -1)[:, sizes[0]:sizes[0] + sizes[1]]
    d9 = jnp.concatenate([lax.dynamic_slice(dmodc_t[None], (0, me * n_mod), (1, n_mod)),
                          lax.dynamic_slice(dmx_all, (0, me * n_mod), (N_DEV, n_mod)),
                          jnp.zeros((16 - 1 - N_DEV, n_mod), F32)], axis=0)
    g_w_mod, dcc_part = _mod_bwd(a9, d9, w_mod[0])
    dcc_all, = _all_gather([dcc_part], "ag_c_ctx")
    dcc_all = send_w_in(2, dcc_all)
    g_c_ctx = _sum_devices(dcc_all)[0]

    recv = {}
    for first, leaves, handle in sent:
        if not first.startswith("w_in"):
            recv.update(zip(leaves, _split_wait(handle, "grads_%s_wait" % first, g_c_ctx)))
    moms = [(m_w_in, v_w_in), (m_w_br_hg, v_w_br_hg), (m_w_br_gla, v_w_br_gla), (m_w_out, v_w_out),
            (m_w_ff_gate, v_w_ff_gate), (m_w_ff_up, v_w_ff_up), (m_w_ff_down, v_w_ff_down)]
    names = ["w_in", "w_br_hg", "w_br_gla", "w_out", "w_ff_gate", "w_ff_up", "w_ff_down"]
    res = {}

    def update(nm, w, m, v):
        if nm in ("w_ff_gate", "w_ff_up"):
            outs = _adamw(recv[nm], w, tr_(m), tr_(v), "adamw_" + nm)
            res[nm] = [jnp.swapaxes(o, 0, 1)[None] for o in outs]
        else:
            res[nm] = [o[None] for o in _adamw(recv[nm], w, m[0], v[0], "adamw_" + nm)]

    for nm, w, (m, v) in list(zip(names, big, moms))[1:]:
        update(nm, w, m, v)
    res["w_mod"] = [o[None] for o in _adamw(g_w_mod[None], w_mod[0], m_w_mod[0], v_w_mod[0], "adamw_w_mod")]

    small = [("c_ctx", c_ctx, m_c_ctx, v_c_ctx, g_c_ctx), ("b_mod", b_mod, m_b_mod, v_b_mod, g_b_mod),
             ("norm_pre1", norm_pre1, m_norm_pre1, v_norm_pre1, g_norms[0]),
             ("norm_post1", norm_post1, m_norm_post1, v_norm_post1, g_norms[1]),
             ("norm_pre2", norm_pre2, m_norm_pre2, v_norm_pre2, g_norms[2]),
             ("norm_post2", norm_post2, m_norm_post2, v_norm_post2, g_norms[3]),
             ("hg_lb", hg_lb, m_hg_lb, v_hg_lb, g_hg_lb), ("hg_onorm", hg_onorm, m_hg_onorm, v_hg_onorm, g_hg_on),
             ("gla_w_gk", gla_w_gk, m_gla_w_gk, v_gla_w_gk, g_wgk), ("gla_b_gk", gla_b_gk, m_gla_b_gk, v_gla_b_gk, g_bgk),
             ("gla_onorm", gla_onorm, m_gla_onorm, v_gla_onorm, g_gla_on)]
    flat = lambda k: jnp.concatenate([s[k].reshape(-1) for s in small]).reshape(-1, DH)
    outs = _adamw(flat(4)[None], flat(1), flat(2), flat(3), "adamw_small")
    off = 0
    for nm, w, _, _, _ in small:
        res[nm] = [o.reshape(-1)[off:off + w.size].reshape(w.shape) for o in outs]
        off += w.size

    done = [res[nm][0] for nm in names[1:]] + [res["w_mod"][0], outs[0]]
    sums = []
    for i, (first, leaves, handle) in enumerate(s for s in sent if s[0].startswith("w_in")):
        half = W_IN_GRAD_CHUNKS[i][0]
        land, = _split_wait(handle, "grads_%s_wait" % first, done, srcs=w_in_grad[half])
        w_in_grad[half] = handle["srcs"]
        sums.append(_sum_windows(land, "sum_windows%d" % i))
    recv["w_in"] = jnp.concatenate(sums, axis=0)[None, :, :W_IN_SHARD]
    update("w_in", big[0], *moms[0])

    order = ["c_ctx", "w_mod", "b_mod", "norm_pre1", "norm_post1", "norm_pre2", "norm_post2", "w_in", "hg_lb",
             "hg_onorm", "gla_w_gk", "gla_b_gk", "gla_onorm", "w_br_hg", "w_br_gla", "w_out", "w_ff_gate", "w_ff_up",
             "w_ff_down"]
    return (loss, grad_x, *[res[n][k] for k in range(4) for n in order])
```

```python
import functools

import jax
import jax.numpy as jnp
from jax import lax
from jax.experimental import pallas as pl
from jax.experimental.pallas import tpu as pltpu

F32 = jnp.float32
BF16 = jnp.bfloat16
HI = lax.Precision.HIGHEST

N_DEV = 8
D = 1024
CTX = 256
HW = 512
DH = 128
NH = 8
D_FF = 2816
EPS = 1e-6
GLA_NORM = 16.0
CHUNK = 64
TR = 256
NCT = CTX // TR
W_IN_COLS = 7168
MAIN0 = 0
LR0 = 4608
GW = 1152
GOFF = 32
GATE_HG0 = LR0
GATE_GLA0 = LR0 + D
LEVELS = (32, 16, 8)
EXP_CLAMP = 80.0
VMEM_LIMIT = 48 * 1024 * 1024

ADAM_LR, ADAM_B1, ADAM_B2, ADAM_EPS, ADAM_WD, ADAM_STEP = 0.001, 0.9, 0.999, 1e-08, 0.01, 10


def _cp(*sem):
    return pltpu.CompilerParams(dimension_semantics=sem, vmem_limit_bytes=VMEM_LIMIT)


def _sig(x):
    return jax.nn.sigmoid(x)


def _silu(x):
    return x * _sig(x)


def _dsilu(x):
    s = _sig(x)
    return s * (1.0 + x * (1.0 - s))


def _rstd(x):
    return lax.rsqrt(jnp.mean(x * x, axis=-1, keepdims=True) + EPS)


def _rms_bwd(a, y, r):
    return r * (a - y * (r * r) * jnp.mean(a * y, axis=-1, keepdims=True))


def _colsum(x):
    return jnp.sum(x, axis=0, keepdims=True)


def _dot(a, b, dims, precision=None):
    return lax.dot_general(a, b, (dims, ((), ())), preferred_element_type=F32, precision=precision)


NN = ((1,), (0,))
NT = ((1,), (1,))
TN = ((0,), (0,))

SCAN_HEADS_FWD = 4
SCAN_HEADS_BWD = 4


def _split_dot(m, x):
    mb = m.astype(BF16)
    x1 = x.astype(BF16)
    r1 = x - x1.astype(F32)
    x2 = r1.astype(BF16)
    x3 = (r1 - x2.astype(F32)).astype(BF16)
    return _dot(mb, x1, NN) + _dot(mb, x2, NN) + _dot(mb, x3, NN)


def _matmul(a, b, dims, out_dtype, name, tm, tn, tk, a_off=0, m_out=None):
    pair = isinstance(b, (tuple, list))
    bs = list(b) if pair else [b]
    b1 = bs[0]
    rows = b1.shape[0] * len(bs)
    half = None
    if dims == NN:
        m, k, n = a.shape[0], rows, b1.shape[1]
        a_spec = pl.BlockSpec((tm, tk), lambda i, j, kk: (i, kk + a_off))
        half = b1.shape[0] // tk
        b_maps = [lambda i, j, kk: (kk, j)] if not pair else [
            lambda i, j, kk: (jnp.minimum(kk, half - 1), j), lambda i, j, kk: (jnp.maximum(kk - half, 0), j)]
        b_specs = [pl.BlockSpec((tk, tn), f) for f in b_maps]
        axis = 2
    elif dims == NT:
        m, k, n = a.shape[0], b1.shape[1], rows
        a_spec = pl.BlockSpec((tm, tk), lambda i, j, kk: (i, kk + a_off))
        half = b1.shape[0] // tn
        b_maps = [lambda i, j, kk: (j, kk)] if not pair else [
            lambda i, j, kk: (jnp.minimum(j, half - 1), kk), lambda i, j, kk: (jnp.maximum(j - half, 0), kk)]
        b_specs = [pl.BlockSpec((tn, tk), f) for f in b_maps]
        axis = 1
    else:
        assert not pair
        m, k = (a.shape[1] if m_out is None else m_out), a.shape[0]
        n = b1.shape[1]
        a_spec = pl.BlockSpec((tk, tm), lambda i, j, kk: (kk, i + a_off))
        b_specs = [pl.BlockSpec((tk, tn), lambda i, j, kk: (kk, j))]
    assert m % tm == 0 and n % tn == 0 and k % tk == 0, (name, m, n, k, tm, tn, tk)
    nk = k // tk
    nb = len(bs)

    def body(a_ref, *refs):
        o_ref = refs[nb]
        if pair:
            bv = jnp.where(pl.program_id(axis) < half, refs[0][...], refs[1][...])
        else:
            bv = refs[0][...]
        part = _dot(a_ref[...], bv, dims)
        if nk == 1:
            o_ref[...] = part.astype(o_ref.dtype)
            return
        acc_ref = refs[nb + 1]
        kk = pl.program_id(2)

        @pl.when(kk == 0)
        def _():
            acc_ref[...] = part

        @pl.when(kk > 0)
        def _():
            acc_ref[...] += part

        @pl.when(kk == nk - 1)
        def _():
            o_ref[...] = acc_ref[...].astype(o_ref.dtype)

    return pl.pallas_call(
        body,
        name=name,
        grid=(m // tm, n // tn, nk),
        in_specs=[a_spec] + b_specs,
        out_specs=pl.BlockSpec((tm, tn), lambda i, j, kk: (i, j)),
        out_shape=jax.ShapeDtypeStruct((m, n), out_dtype),
        scratch_shapes=[] if nk == 1 else [pltpu.VMEM((tm, tn), F32)],
        compiler_params=_cp("parallel", "parallel", "arbitrary"),
    )(a, *bs)


def _row(c):
    return pl.BlockSpec((TR, c), lambda i: (i, 0))


def _rowcol(width, cb):
    return pl.BlockSpec((TR, width), lambda i: (i, cb))


def _full(shape):
    return pl.BlockSpec(shape, lambda i: (0,) * len(shape))


def _mod_row(mc_ref, mx_ref, k, is_ctx):
    return jnp.where(is_ctx, mc_ref[k:k + 1, :], mx_ref[k:k + 1, :])


def _acc_row(ref, k, val):
    ref[k:k + 1, :] += val


def _acc_mod(ref, k, is_ctx, val):
    zero = jnp.zeros_like(val)
    ref[k:k + 1, :] += jnp.where(is_ctx, val, zero)
    ref[k + 1:k + 2, :] += jnp.where(is_ctx, zero, val)


def _prenorm(z, nw, modc, modx, i_shift, i_scale, name):
    t = z.shape[0]

    def body(z_ref, nw_ref, mc_ref, mx_ref, h_ref):
        is_ctx = pl.program_id(0) < NCT
        x = z_ref[...]
        n = x * _rstd(x) * nw_ref[...]
        h = n * (1.0 + _mod_row(mc_ref, mx_ref, i_scale, is_ctx)) + _mod_row(mc_ref, mx_ref, i_shift, is_ctx)
        h_ref[...] = h.astype(BF16)

    return pl.pallas_call(
        body, name=name, grid=(t // TR,),
        in_specs=[_row(D), _full((1, D)), _full((8, D)), _full((8, D))],
        out_specs=_row(D),
        out_shape=jax.ShapeDtypeStruct((t, D), BF16),
        compiler_params=_cp("parallel"),
    )(z, nw, modc, modx)


def _hg_lb(lb_ref, d):
    a0 = lb_ref[0, d:d + 1, :]
    a1 = lb_ref[1, d:d + 1, :]
    mx = jnp.maximum(a0, a1)
    e0 = jnp.exp(a0 - mx)
    e1 = jnp.exp(a1 - mx)
    return e0 / (e0 + e1)


def _log_sigmoid(x):
    return jnp.minimum(x, 0.0) - jnp.log(1.0 + jnp.exp(-jnp.abs(x)))


def _gates_fwd(p, hg_lb, wgk, bgk):
    t = p.shape[0]
    seg = lambda j: _rowcol(HW, MAIN0 // HW + j)

    def body(hq_ref, hi_ref, hf_ref, hb_ref, gq_ref, gk_ref, gv_ref, lr_ref, lb_ref, wgk_ref, bgk_ref,
             q_ref, v_ref, kf_ref, kb_ref, gf_ref, gb_ref):
        q_ref[:, :HW] = _silu(hq_ref[...].astype(F32))
        q_ref[:, HW:] = gq_ref[...].astype(F32) * (DH ** -0.5)
        v_ref[:, :HW] = hi_ref[...].astype(F32)
        v_ref[:, HW:] = gv_ref[...].astype(F32)
        xg = _dot(lr_ref[...].astype(BF16), wgk_ref[...], NN) + bgk_ref[...]
        for d, (raw_ref, k_ref, g_ref) in enumerate(((hf_ref, kf_ref, gf_ref), (hb_ref, kb_ref, gb_ref))):
            lbd = _hg_lb(lb_ref, d)
            f = lbd + (1.0 - lbd) * _sig(raw_ref[...].astype(F32))
            k_ref[:, :HW] = 1.0 - f
            k_ref[:, HW:] = gk_ref[...].astype(F32)
            g_ref[:, :HW] = jnp.log(f)
            g_ref[:, HW:] = _log_sigmoid(xg[:, d * HW:(d + 1) * HW]) * (1.0 / GLA_NORM)

    out = jax.ShapeDtypeStruct((t, D), F32)
    return pl.pallas_call(
        body, name="gates_fwd", grid=(t // TR,),
        in_specs=[seg(0), seg(1), seg(2), seg(3), seg(5), seg(6), seg(7), _rowcol(DH, LR0 // DH),
                  _full((2, 2, HW)), _full((DH, D)), _full((1, D))],
        out_specs=[_row(D)] * 6,
        out_shape=[out] * 6,
        compiler_params=_cp("parallel"),
    )(p, p, p, p, p, p, p, p, hg_lb, wgk, bgk)


def _post_fwd(o_fw, o_bw, p, onw):
    t = o_fw.shape[0]

    def body(of_ref, ob_ref, g1_ref, g2_ref, w_ref, y_ref):
        for h in range(NH):
            sl = slice(h * DH, (h + 1) * DH)
            o = of_ref[:, sl] + ob_ref[:, sl]
            g_ref = g1_ref if h < NH // 2 else g2_ref
            gs = slice((h % (NH // 2)) * DH, (h % (NH // 2) + 1) * DH)
            n = o * _rstd(o) * w_ref[:, sl]
            y_ref[:, sl] = (n * _silu(g_ref[:, gs].astype(F32))).astype(BF16)

    return pl.pallas_call(
        body, name="post_fwd", grid=(t // TR,),
        in_specs=[_row(D), _row(D), _rowcol(HW, MAIN0 // HW + 4), _rowcol(HW, MAIN0 // HW + 8), _full((1, D))],
        out_specs=_row(D),
        out_shape=jax.ShapeDtypeStruct((t, D), BF16),
        compiler_params=_cp("parallel"),
    )(o_fw, o_bw, p, p, onw)


def _gate_window_specs(col0):
    return [_rowcol(HW, col0 // HW), _rowcol(HW, col0 // HW + 1), _rowcol(DH, (col0 + 2 * HW) // DH)]


def _gate_window(refs):
    return jnp.concatenate([r[...].astype(F32) for r in refs], axis=1)


def _merge_fwd(p, u1, u2):
    t = p.shape[0]

    def body(a0, a1, a2, b0, b1, b2, u1_ref, u2_ref, m_ref):
        f = lambda r: r[...].astype(F32)
        m_ref[...] = (_sig(_gate_window((a0, a1, a2))) * f(u1_ref)
                      + _sig(_gate_window((b0, b1, b2))) * f(u2_ref)).astype(BF16)

    return pl.pallas_call(
        body, name="merge_fwd", grid=(t // TR,),
        in_specs=_gate_window_specs(GATE_HG0) + _gate_window_specs(GATE_GLA0) + [_row(GW), _row(GW)],
        out_specs=_row(GW),
        out_shape=jax.ShapeDtypeStruct((t, GW), BF16),
        compiler_params=_cp("parallel"),
    )(p, p, p, p, p, p, u1, u2)


def _mid_fwd(z, y1, nw_post, nw_pre, modc, modx):
    t = z.shape[0]

    def body(z_ref, y_ref, wpo_ref, wpr_ref, mc_ref, mx_ref, z1_ref, h_ref):
        is_ctx = pl.program_id(0) < NCT
        y = y_ref[...]
        z1 = z_ref[...] + _mod_row(mc_ref, mx_ref, 2, is_ctx) * (y * _rstd(y) * wpo_ref[...])
        z1_ref[...] = z1
        n = z1 * _rstd(z1) * wpr_ref[...]
        h = n * (1.0 + _mod_row(mc_ref, mx_ref, 4, is_ctx)) + _mod_row(mc_ref, mx_ref, 3, is_ctx)
        h_ref[...] = h.astype(BF16)

    return pl.pallas_call(
        body, name="mid_fwd", grid=(t // TR,),
        in_specs=[_row(D), _row(D), _full((1, D)), _full((1, D)), _full((8, D)), _full((8, D))],
        out_specs=[_row(D), _row(D)],
        out_shape=[jax.ShapeDtypeStruct((t, D), F32), jax.ShapeDtypeStruct((t, D), BF16)],
        compiler_params=_cp("parallel"),
    )(z, y1, nw_post, nw_pre, modc, modx)


def _swiglu_fwd(uv):
    t = uv.shape[0]

    def body(u_ref, v_ref, a_ref):
        a_ref[...] = (_silu(u_ref[...].astype(F32)) * v_ref[...].astype(F32)).astype(BF16)

    return pl.pallas_call(
        body, name="swiglu_fwd", grid=(t // TR,),
        in_specs=[_rowcol(D_FF, 0), _rowcol(D_FF, 1)],
        out_specs=_row(D_FF),
        out_shape=jax.ShapeDtypeStruct((t, D_FF), BF16),
        compiler_params=_cp("parallel"),
    )(uv, uv)


def _swiglu_bwd(uv, da):
    t = uv.shape[0]

    def body(u_ref, v_ref, da_ref, d_ref):
        u = u_ref[...].astype(F32)
        d = da_ref[...].astype(F32)
        d_ref[:, :D_FF] = (d * v_ref[...].astype(F32) * _dsilu(u)).astype(BF16)
        d_ref[:, D_FF:] = (d * _silu(u)).astype(BF16)

    return pl.pallas_call(
        body, name="swiglu_bwd", grid=(t // TR,),
        in_specs=[_rowcol(D_FF, 0), _rowcol(D_FF, 1), _row(D_FF)],
        out_specs=_row(2 * D_FF),
        out_shape=jax.ShapeDtypeStruct((t, 2 * D_FF), BF16),
        compiler_params=_cp("parallel"),
    )(uv, uv, da)


def _final(z1, y2, target, nw, modc, modx):
    t = z1.shape[0]

    def body(z1_ref, y_ref, tg_ref, w_ref, mc_ref, mx_ref, dz_ref, dy_ref, loss_ref, sm_ref):
        i = pl.program_id(0)
        is_ctx = i < NCT

        @pl.when(i == 0)
        def _():
            loss_ref[...] = jnp.zeros_like(loss_ref)
            sm_ref[...] = jnp.zeros_like(sm_ref)

        g = _mod_row(mc_ref, mx_ref, 5, is_ctx)
        y = y_ref[...]
        r = _rstd(y)
        w = w_ref[...]
        yr = y * r
        n = yr * w
        e = z1_ref[...] + g * n - tg_ref[...]
        lat = jnp.where(is_ctx, 0.0, 1.0)
        loss_ref[...] += lat * _colsum(e * e)
        dz = e * (lat / D)
        dz_ref[...] = dz
        _acc_mod(sm_ref, 0, is_ctx, _colsum(dz * n))
        dn = dz * g
        _acc_row(sm_ref, 2, _colsum(dn * yr))
        dy_ref[...] = _rms_bwd(dn * w, y, r).astype(BF16)

    return pl.pallas_call(
        body, name="final", grid=(t // TR,),
        in_specs=[_row(D), _row(D), pl.BlockSpec((TR, D), lambda i: (jnp.maximum(i - NCT, 0), 0)),
                  _full((1, D)), _full((8, D)), _full((8, D))],
        out_specs=[_row(D), _row(D), _full((1, D)), _full((8, D))],
        out_shape=[jax.ShapeDtypeStruct((t, D), F32), jax.ShapeDtypeStruct((t, D), BF16),
                   jax.ShapeDtypeStruct((1, D), F32), jax.ShapeDtypeStruct((8, D), F32)],
        compiler_params=_cp("arbitrary"),
    )(z1, y2, target, nw, modc, modx)


def _mid_bwd(dh2, dz, z, z1, y1, nw_post, nw_pre, modc, modx):
    t = z.shape[0]

    def body(dh_ref, dz_ref, z_ref, z1_ref, y_ref, wpo_ref, wpr_ref, mc_ref, mx_ref, dzo_ref, dy_ref, sm_ref):
        i = pl.program_id(0)
        is_ctx = i < NCT

        @pl.when(i == 0)
        def _():
            sm_ref[...] = jnp.zeros_like(sm_ref)

        dh = dh_ref[...]
        z1 = z1_ref[...]
        r = _rstd(z1)
        zr = z1 * r
        wpr = wpr_ref[...]
        n = zr * wpr
        _acc_mod(sm_ref, 0, is_ctx, _colsum(dh))
        _acc_mod(sm_ref, 2, is_ctx, _colsum(dh * n))
        dn = dh * (1.0 + _mod_row(mc_ref, mx_ref, 4, is_ctx))
        _acc_row(sm_ref, 6, _colsum(dn * zr))
        dz1 = dz_ref[...] + _rms_bwd(dn * wpr, z1, r)
        dzo_ref[...] = dz1
        y = y_ref[...]
        r1 = _rstd(y)
        yr = y * r1
        wpo = wpo_ref[...]
        g = _mod_row(mc_ref, mx_ref, 2, is_ctx)
        _acc_mod(sm_ref, 4, is_ctx, _colsum(dz1 * (yr * wpo)))
        dn1 = dz1 * g
        _acc_row(sm_ref, 7, _colsum(dn1 * yr))
        dy_ref[...] = _rms_bwd(dn1 * wpo, y, r1).astype(BF16)

    return pl.pallas_call(
        body, name="mid_bwd", grid=(t // TR,),
        in_specs=[_row(D)] * 5 + [_full((1, D)), _full((1, D)), _full((8, D)), _full((8, D))],
        out_specs=[_row(D), _row(D), _full((8, D))],
        out_shape=[jax.ShapeDtypeStruct((t, D), F32), jax.ShapeDtypeStruct((t, D), BF16),
                   jax.ShapeDtypeStruct((8, D), F32)],
        compiler_params=_cp("arbitrary"),
    )(dh2, dz, z, z1, y1, nw_post, nw_pre, modc, modx)


def _pre_bwd(dh1, dz, z, nw, modc, modx):
    t = z.shape[0]

    def body(dh_ref, dz_ref, z_ref, w_ref, mc_ref, mx_ref, dzo_ref, sm_ref):
        i = pl.program_id(0)
        is_ctx = i < NCT

        @pl.when(i == 0)
        def _():
            sm_ref[...] = jnp.zeros_like(sm_ref)

        dh = dh_ref[...]
        x = z_ref[...]
        r = _rstd(x)
        xr = x * r
        w = w_ref[...]
        _acc_mod(sm_ref, 0, is_ctx, _colsum(dh))
        _acc_mod(sm_ref, 2, is_ctx, _colsum(dh * (xr * w)))
        dn = dh * (1.0 + _mod_row(mc_ref, mx_ref, 1, is_ctx))
        _acc_row(sm_ref, 4, _colsum(dn * xr))
        dzo_ref[...] = dz_ref[...] + _rms_bwd(dn * w, x, r)

    return pl.pallas_call(
        body, name="pre_bwd", grid=(t // TR,),
        in_specs=[_row(D)] * 3 + [_full((1, D)), _full((8, D)), _full((8, D))],
        out_specs=[pl.BlockSpec((TR, D), lambda i: (jnp.maximum(i - NCT, 0), 0)), _full((8, D))],
        out_shape=[jax.ShapeDtypeStruct((t - CTX, D), F32), jax.ShapeDtypeStruct((8, D), F32)],
        compiler_params=_cp("arbitrary"),
    )(dh1, dz, z, nw, modc, modx)


def _merge_bwd(dm, p, u1, u2):
    t = dm.shape[0]

    def body(dm_ref, a0, a1, a2, b0, b1, b2, u1_ref, u2_ref, du1_ref, du2_ref, dg_ref):
        dm_ = dm_ref[...]
        s1 = _sig(_gate_window((a0, a1, a2)))
        s2 = _sig(_gate_window((b0, b1, b2)))
        du1_ref[...] = (dm_ * s1).astype(BF16)
        du2_ref[...] = (dm_ * s2).astype(BF16)
        dg_ref[:, :GW] = (dm_ * u1_ref[...].astype(F32) * s1 * (1.0 - s1)).astype(BF16)
        dg_ref[:, GW:] = (dm_ * u2_ref[...].astype(F32) * s2 * (1.0 - s2)).astype(BF16)

    return pl.pallas_call(
        body, name="merge_bwd", grid=(t // TR,),
        in_specs=[_row(GW)] + _gate_window_specs(GATE_HG0) + _gate_window_specs(GATE_GLA0) + [_row(GW), _row(GW)],
        out_specs=[_row(GW), _row(GW), _row(2 * GW)],
        out_shape=[jax.ShapeDtypeStruct((t, GW), BF16), jax.ShapeDtypeStruct((t, GW), BF16),
                   jax.ShapeDtypeStruct((t, 2 * GW), BF16)],
        compiler_params=_cp("parallel"),
    )(dm, p, p, p, p, p, p, u1, u2)


def _post_bwd(dy_hg, dy_gla, o_fw, o_bw, p, onw):
    t = o_fw.shape[0]

    def body(d1_ref, d2_ref, of_ref, ob_ref, g1_ref, g2_ref, w_ref, do_ref, dg_ref, sm_ref):
        @pl.when(pl.program_id(0) == 0)
        def _():
            sm_ref[...] = jnp.zeros_like(sm_ref)

        for h in range(NH):
            sl = slice(h * DH, (h + 1) * DH)
            gs = slice((h % (NH // 2)) * DH, (h % (NH // 2) + 1) * DH)
            g_ref, d_ref = (g1_ref, d1_ref) if h < NH // 2 else (g2_ref, d2_ref)
            o = of_ref[:, sl] + ob_ref[:, sl]
            r = _rstd(o)
            orr = o * r
            w = w_ref[:, sl]
            gt = g_ref[:, gs].astype(F32)
            dy = d_ref[:, gs]
            dg_ref[:, sl] = (dy * (orr * w) * _dsilu(gt)).astype(BF16)
            dn = dy * _silu(gt)
            sm_ref[0:1, sl] += _colsum(dn * orr)
            do_ref[:, sl] = _rms_bwd(dn * w, o, r)

    return pl.pallas_call(
        body, name="post_bwd", grid=(t // TR,),
        in_specs=[_row(HW), _row(HW), _row(D), _row(D), _rowcol(HW, MAIN0 // HW + 4), _rowcol(HW, MAIN0 // HW + 8),
                  _full((1, D))],
        out_specs=[_row(D), _row(D), _full((8, D))],
        out_shape=[jax.ShapeDtypeStruct((t, D), F32), jax.ShapeDtypeStruct((t, D), BF16),
                   jax.ShapeDtypeStruct((8, D), F32)],
        compiler_params=_cp("arbitrary"),
    )(dy_hg, dy_gla, o_fw, o_bw, p, p, onw)


def _gates_bwd(p, hg_lb, wgk, bgk, dgm, dgo, dq_f, dq_b, dv_f, dv_b, dk_f, dk_b, dg_f, dg_b):
    t = p.shape[0]
    seg = lambda j: _rowcol(HW, MAIN0 // HW + j)

    def body(hq_ref, hf_ref, hb_ref, lr_ref, lb_ref, wgk_ref, bgk_ref, dgm_ref, dgo_ref,
             dqf_ref, dqb_ref, dvf_ref, dvb_ref, dkf_ref, dkb_ref, dgf_ref, dgb_ref,
             dp_ref, dlb_ref, dw_ref, db_ref):
        @pl.when(pl.program_id(0) == 0)
        def _():
            dlb_ref[...] = jnp.zeros_like(dlb_ref)
            dw_ref[...] = jnp.zeros_like(dw_ref)
            db_ref[...] = jnp.zeros_like(db_ref)

        c0 = MAIN0

        def put(j, val):
            dp_ref[:, c0 + j * HW:c0 + (j + 1) * HW] = val.astype(BF16)

        dq = dqf_ref[...] + dqb_ref[...]
        dv = dvf_ref[...] + dvb_ref[...]
        put(0, dq[:, :HW] * _dsilu(hq_ref[...].astype(F32)))
        put(1, dv[:, :HW])
        put(5, dq[:, HW:] * (DH ** -0.5))
        put(7, dv[:, HW:])
        put(6, dkf_ref[:, HW:] + dkb_ref[:, HW:])
        dp_ref[:, c0 + 4 * HW:c0 + 5 * HW] = dgo_ref[:, :HW]
        dp_ref[:, c0 + 8 * HW:c0 + 9 * HW] = dgo_ref[:, HW:]
        lr = lr_ref[...].astype(BF16)
        xg = _dot(lr, wgk_ref[...], NN) + bgk_ref[...]
        dxg = []
        for d, (raw_ref, dk_ref, dg_ref) in enumerate(((hf_ref, dkf_ref, dgf_ref), (hb_ref, dkb_ref, dgb_ref))):
            lbd = _hg_lb(lb_ref, d)
            s = _sig(raw_ref[...].astype(F32))
            f = lbd + (1.0 - lbd) * s
            df = dg_ref[:, :HW] / f - dk_ref[:, :HW]
            put(2 + d, df * (1.0 - lbd) * s * (1.0 - s))
            dlb_ref[d:d + 1, :] += _colsum(df * (1.0 - s)) * (lbd * (1.0 - lbd))
            dxg.append(dg_ref[:, HW:] * (1.0 / GLA_NORM) * _sig(-xg[:, d * HW:(d + 1) * HW]))
        dxg = jnp.concatenate(dxg, axis=1)
        db_ref[0:1, :] += _colsum(dxg)
        dxg_b = dxg.astype(BF16)
        dw_ref[...] += _dot(lr, dxg_b, TN)
        dlr = _dot(dxg_b, wgk_ref[...], NT)
        dp_ref[:, LR0:LR0 + DH] = (dlr + dgm_ref[:, :DH].astype(F32)).astype(BF16)
        dp_ref[:, LR0 + DH:GATE_GLA0] = dgm_ref[:, DH:D]
        dp_ref[:, GATE_GLA0:GATE_GLA0 + DH] = dgm_ref[:, D:GW] + dgm_ref[:, GW:GW + DH]
        dp_ref[:, GATE_GLA0 + DH:GATE_GLA0 + GW] = dgm_ref[:, GW + DH:]
        dp_ref[:, GATE_GLA0 + GW:] = jnp.zeros((TR, W_IN_COLS - GATE_GLA0 - GW), BF16)

    return pl.pallas_call(
        body, name="gates_bwd", grid=(t // TR,),
        in_specs=[seg(0), seg(2), seg(3), _rowcol(DH, LR0 // DH), _full((2, 2, HW)), _full((DH, D)), _full((1, D)),
                  _row(2 * GW), _row(D)] + [_row(D)] * 8,
        out_specs=[_row(W_IN_COLS), _full((8, HW)), _full((DH, D)), _full((8, D))],
        out_shape=[jax.ShapeDtypeStruct((t, W_IN_COLS), BF16), jax.ShapeDtypeStruct((8, HW), F32),
                   jax.ShapeDtypeStruct((DH, D), F32), jax.ShapeDtypeStruct((8, D), F32)],
        compiler_params=_cp("arbitrary"),
    )(p, p, p, p, hg_lb, wgk, bgk, dgm, dgo, dq_f, dq_b, dv_f, dv_b, dk_f, dk_b, dg_f, dg_b)


def _scan_consts(rev):
    r = lax.broadcasted_iota(jnp.int32, (CHUNK, CHUNK), 0)
    u = lax.broadcasted_iota(jnp.int32, (CHUNK, CHUNK), 1)
    rp = lax.broadcasted_iota(jnp.int32, (CHUNK, 1), 0)
    if rev:
        r, u, rp = CHUNK - 1 - r, CHUNK - 1 - u, CHUNK - 1 - rp
    tri = jnp.where(u <= r, 1.0, 0.0).astype(F32)
    tri_t = jnp.where(r <= u, 1.0, 0.0).astype(F32)
    lv = []
    for b in LEVELS:
        sh = b.bit_length() - 1
        pair = ((r >> sh) == (u >> sh) + 1) & (((u >> sh) & 1) == 0)
        pair_t = ((u >> sh) == (r >> sh) + 1) & (((r >> sh) & 1) == 0)
        tside = ((rp >> sh) & 1) == 1
        lv.append((pair, pair_t, tside))
    bd = LEVELS[-1].bit_length() - 1
    diag = ((r >> bd) == (u >> bd)) & (u <= r)
    diag_t = ((r >> bd) == (u >> bd)) & (r <= u)
    return tri, tri_t, lv, diag, diag_t


def _row_of(pos, rev):
    return CHUNK - 1 - pos if rev else pos


def _chunk_terms(cum, b_scr, consts, rev):
    _, _, lv, _, _ = consts
    terms = []
    for b, (_, _, tside) in zip(LEVELS, lv):
        pieces = []
        for j in range(CHUNK // (2 * b)):
            row = _row_of(2 * b * j + b - 1, rev)
            pieces.append(jnp.broadcast_to(b_scr[row:row + 1, :], (2 * b, DH)))
        if rev:
            pieces = pieces[::-1]
        bnd = pieces[0] if len(pieces) == 1 else jnp.concatenate(pieces, axis=0)
        w = jnp.exp(jnp.minimum(jnp.where(tside, cum - bnd, bnd - cum), 0.0))
        wq = jnp.where(tside, w, 0.0)
        wk = jnp.where(tside, 0.0, w)
        terms.append((wq, wk))
    b = LEVELS[-1]
    pieces = []
    for j in range(CHUNK // b):
        if j == 0:
            pieces.append(jnp.zeros((b, DH), F32))
        else:
            row = _row_of(b * j - 1, rev)
            pieces.append(jnp.broadcast_to(b_scr[row:row + 1, :], (b, DH)))
    if rev:
        pieces = pieces[::-1]
    start = jnp.concatenate(pieces, axis=0)
    wq = jnp.exp(jnp.minimum(cum - start, 0.0))
    wk = jnp.exp(jnp.minimum(start - cum, EXP_CLAMP))
    terms.append((wq, wk))
    return terms


def _run_staged(units):
    live = list(units)
    while live:
        nxt = []
        for u in live:
            try:
                next(u)
                nxt.append(u)
            except StopIteration:
                pass
        live = nxt


SCAN_TB = 256
SCAN_CB = SCAN_TB // CHUNK


def _block_order(i, ntb, rev):
    nctx = CTX // SCAN_TB
    if not rev:
        return i
    return jnp.where(i < nctx, nctx - 1 - i, ntb - 1 - (i - nctx))


def _chunk_in_block(j, rev):
    return SCAN_CB - 1 - j if rev else j


def _scan_fwd(q, k, v, g, rev):
    t = q.shape[0]
    nc = t // CHUNK
    hpb = SCAN_HEADS_FWD

    def body(q_ref, k_ref, v_ref, g_ref, o_ref, st_ref, s_scr, b_scr):
        consts = _scan_consts(rev)
        _, _, lv, diag, _ = consts
        masks = [pair for pair, _, _ in lv] + [diag]

        @pl.when(pl.program_id(1) == 0)
        def _():
            s_scr[...] = jnp.zeros_like(s_scr)

        tri = consts[0]
        state = {hh: s_scr[hh] for hh in range(hpb)}

        def unit(hh, j):
            sl = slice(hh * DH, (hh + 1) * DH)
            c = _chunk_in_block(j, rev)
            rows = slice(c * CHUNK, (c + 1) * CHUNK)
            b_ref = b_scr.at[hh * SCAN_CB + j]
            qc, kc, vc, gc = q_ref[rows, sl], k_ref[rows, sl], v_ref[rows, sl], g_ref[rows, sl]
            cum = _split_dot(tri, gc)
            b_ref[...] = cum
            yield
            terms = _chunk_terms(cum, b_ref, consts, rev)
            ops = [((qc * wq).astype(BF16), (kc * wk).astype(BF16)) for wq, wk in terms]
            tot = _colsum(gc)
            qe = (qc * jnp.exp(cum)).astype(BF16)
            ke = (kc * jnp.exp(tot - cum)).astype(BF16)
            vb = vc.astype(BF16)
            yield
            scs = [_dot(qt, kt, NT) for qt, kt in ops]
            kv = _dot(vb, ke, TN)
            yield
            a = jnp.zeros((CHUNK, CHUNK), F32)
            for sc, m in zip(scs, masks):
                a = a + jnp.where(m, sc, 0.0)
            o_intra = _dot(a.astype(BF16), vb, NN)
            yield
            st = state[hh]
            st_ref[hh, c] = st
            o_ref[rows, sl] = o_intra + _dot(qe, st.astype(BF16), NT)
            state[hh] = st * jnp.exp(tot) + kv
            yield

        _run_staged([unit(hh, j) for hh in range(hpb) for j in range(SCAN_CB)])
        for hh in range(hpb):
            s_scr[hh] = state[hh]

    ntb = t // SCAN_TB
    col = pl.BlockSpec((SCAN_TB, hpb * DH), lambda h, i: (_block_order(i, ntb, rev), h))
    return pl.pallas_call(
        body, name="scan_fwd_" + ("bw" if rev else "fw"), grid=(NH // hpb, ntb),
        in_specs=[col] * 4,
        out_specs=[col, pl.BlockSpec((hpb, SCAN_CB, DH, DH), lambda h, i: (h, _block_order(i, ntb, rev), 0, 0))],
        out_shape=[jax.ShapeDtypeStruct((t, D), F32), jax.ShapeDtypeStruct((NH, nc, DH, DH), F32)],
        scratch_shapes=[pltpu.VMEM((hpb, DH, DH), F32), pltpu.VMEM((hpb * SCAN_CB, CHUNK, DH), F32)],
        compiler_params=_cp("parallel", "arbitrary"),
    )(q, k, v, g)


def _scan_bwd(q, k, v, g, do, states, rev):
    t = q.shape[0]
    nc = t // CHUNK
    hpb = SCAN_HEADS_BWD

    def body(q_ref, k_ref, v_ref, g_ref, do_ref, st_ref, dq_ref, dk_ref, dv_ref, dg_ref, ds_scr, b_scr):
        consts = _scan_consts(rev)
        _, tri_t, lv, diag, diag_t = consts
        masks = [(pair, pair_t) for pair, pair_t, _ in lv] + [(diag, diag_t)]
        @pl.when(pl.program_id(1) == 0)
        def _():
            ds_scr[...] = jnp.zeros_like(ds_scr)

        tri = consts[0]
        dstate = {hh: ds_scr[hh] for hh in range(hpb)}

        def unit(hh, jj):
            sl = slice(hh * DH, (hh + 1) * DH)
            c = _chunk_in_block(SCAN_CB - 1 - jj, rev)
            rows = slice(c * CHUNK, (c + 1) * CHUNK)
            b_ref = b_scr.at[hh * SCAN_CB + jj]
            qc, kc, vc, gc = q_ref[rows, sl], k_ref[rows, sl], v_ref[rows, sl], g_ref[rows, sl]
            dob = do_ref[rows, sl].astype(BF16)
            vb = vc.astype(BF16)
            cum = _split_dot(tri, gc)
            b_ref[...] = cum
            da = _dot(dob, vb, NT)
            da_t = _dot(vb, dob, NT)
            yield
            terms = _chunk_terms(cum, b_ref, consts, rev)
            ops = [((qc * wq).astype(BF16), (kc * wk).astype(BF16)) for wq, wk in terms]
            tot = _colsum(gc)
            e_tot = jnp.exp(tot)
            e_b = jnp.exp(cum)
            e_t = jnp.exp(tot - cum)
            qeb = (qc * e_b).astype(BF16)
            keb = (kc * e_t).astype(BF16)
            dal = [(jnp.where(m, da, 0.0).astype(BF16), jnp.where(m_t, da_t, 0.0).astype(BF16)) for m, m_t in masks]
            yield
            ats = [_dot(ktb, qtb, NT) for qtb, ktb in ops]
            dqts = [_dot(d, ktb, NN) for (d, _), (_, ktb) in zip(dal, ops)]
            dkts = [_dot(d_t, qtb, NN) for (_, d_t), (qtb, _) in zip(dal, ops)]
            qd = _dot(dob, qeb, TN)
            yield
            a_t = jnp.zeros((CHUNK, CHUNK), F32)
            dq = jnp.zeros((CHUNK, DH), F32)
            dk = jnp.zeros((CHUNK, DH), F32)
            db = jnp.zeros((CHUNK, DH), F32)
            for at, dqt, dkt, (wq, wk), (qtb, ktb), (_, m_t) in zip(ats, dqts, dkts, terms, ops, masks):
                a_t = a_t + jnp.where(m_t, at, 0.0)
                dq = dq + dqt * wq
                dk = dk + dkt * wk
                db = db + dqt * qtb.astype(F32) - dkt * ktb.astype(F32)
            dv_intra = _dot(a_t.astype(BF16), dob, NN)
            st = st_ref[hh, c]
            stb = st.astype(BF16)
            dqe = _dot(dob, stb, NN)
            yield
            dst = dstate[hh]
            dstb = dst.astype(BF16)
            dstate[hh] = dst * e_tot + qd
            dv_ref[rows, sl] = dv_intra + _dot(keb, dstb, NT)
            dke = _dot(vb, dstb, NN)
            yield
            qe = qeb.astype(F32)
            ke = keb.astype(F32)
            dq_ref[rows, sl] = dq + dqe * e_b
            dk_ref[rows, sl] = dk + dke * e_t
            db = db + dqe * qe - dke * ke
            dtot = _colsum(dstb.astype(F32) * stb.astype(F32)) * e_tot + _colsum(dke * ke)
            dg_ref[rows, sl] = _split_dot(tri_t, db) + dtot
            yield

        _run_staged([unit(hh, jj) for hh in range(hpb) for jj in range(SCAN_CB)])
        for hh in range(hpb):
            ds_scr[hh] = dstate[hh]

    ntb = t // SCAN_TB
    blk = lambda i: _block_order(ntb - 1 - i, ntb, rev)
    col = pl.BlockSpec((SCAN_TB, hpb * DH), lambda h, i: (blk(i), h))
    out = jax.ShapeDtypeStruct((t, D), F32)
    return pl.pallas_call(
        body, name="scan_bwd_" + ("bw" if rev else "fw"), grid=(NH // hpb, ntb),
        in_specs=[col] * 5 + [pl.BlockSpec((hpb, SCAN_CB, DH, DH), lambda h, i: (h, blk(i), 0, 0))],
        out_specs=[col] * 4,
        out_shape=[out] * 4,
        scratch_shapes=[pltpu.VMEM((hpb, DH, DH), F32), pltpu.VMEM((hpb * SCAN_CB, CHUNK, DH), F32)],
        compiler_params=_cp("parallel", "arbitrary"),
    )(q, k, v, g, do, states)


W_IN_GRAD_CHUNKS = (("a", (0, 512)), ("b", (0, 128)), ("b", (128, 512)))
W_IN_REF = 6688


def _layout_w_in(w):
    return jnp.pad(w, ((0, 0), (0, W_IN_COLS - W_IN_REF)))


def _unlayout_w_in(d):
    return d[:, :W_IN_REF]


def _gate_cols(w):
    return jnp.pad(w, ((0, 0), (GOFF, GW - GOFF - D)))


def _gate_rows(w):
    return jnp.pad(w, ((GOFF, GW - GOFF - D), (0, 0)))


def _layout_wgk(w):
    r = w.shape[1]
    top = jnp.concatenate([w[0], jnp.zeros_like(w[0])], axis=1)
    bot = jnp.concatenate([jnp.zeros_like(w[1]), w[1]], axis=1)
    return jnp.concatenate([top, bot, jnp.zeros((DH - 2 * r, D), w.dtype)], axis=0)


def _unlayout_wgk(d, r=16):
    return jnp.stack([d[:r, :HW], d[r:2 * r, HW:]])


def _local_step(z, target, modc, modx, norms, onw, hg_lb, wgk, bgk, w_in, get_mix, get_ffn, send):
    n_pre1, n_post1, n_pre2, n_post2 = norms
    t = z.shape[0]
    tm = 768 if t % 768 == 0 else 256
    h1 = _prenorm(z, n_pre1, modc, modx, 0, 1, "prenorm1")
    p = _matmul(h1, w_in, NN, BF16, "mm_in", tm, 512, D)
    q, v, k_f, k_b, g_f, g_b = _gates_fwd(p, hg_lb, wgk, bgk)
    o_f, st_f = _scan_fwd(q, k_f, v, g_f, False)
    o_b, st_b = _scan_fwd(q, k_b, v, g_b, True)
    y = _post_fwd(o_f, o_b, p, onw)
    w_br_hg, w_br_gla, w_out = get_mix(y)
    u1 = _matmul(y, w_br_hg, NN, BF16, "mm_br_hg", tm, GW, HW, a_off=0)
    u2 = _matmul(y, w_br_gla, NN, BF16, "mm_br_gla", tm, GW, HW, a_off=1)
    merged = _merge_fwd(p, u1, u2)
    y1 = _matmul(merged, w_out, NN, F32, "mm_out", tm, 512, GW)
    z1, h2 = _mid_fwd(z, y1, n_post1, n_pre2, modc, modx)
    w_gu_t, w_down = get_ffn(h2)
    uv = _matmul(h2, w_gu_t, NT, BF16, "mm_gu", tm, D_FF // 2, D)
    act = _swiglu_fwd(uv)
    y2 = _matmul(act, w_down, NN, F32, "mm_down", tm, 512, D_FF // 2)
    dz, dy2, loss_vec, sm_final = _final(z1, y2, target, n_post2, modc, modx)
    dact = _matmul(dy2, w_down, NT, BF16, "mm_down_dx", tm, D_FF // 2, D)
    d_w_down = _matmul(act, dy2, TN, BF16, "mm_down_dw", D_FF // 2, 512, t)
    duv = _swiglu_bwd(uv, dact)
    dh2 = _matmul(duv, w_gu_t, NN, F32, "mm_gu_dx", tm, 512, D_FF // 2)
    d_w_gate_t = _matmul(duv, h2, TN, BF16, "mm_gate_dw", D_FF // 2, 512, t, a_off=0, m_out=D_FF)
    d_w_up_t = _matmul(duv, h2, TN, BF16, "mm_up_dw", D_FF // 2, 512, t, a_off=2, m_out=D_FF)
    dh2 = send(("w_down", "w_gate_t", "w_up_t"), (d_w_down, d_w_gate_t, d_w_up_t), dh2)
    dz, dy1, sm_mid = _mid_bwd(dh2, dz, z, z1, y1, n_post1, n_pre2, modc, modx)
    dmerged = _matmul(dy1, w_out, NT, F32, "mm_out_dx", tm, GW, D)
    d_w_out = _matmul(merged, dy1, TN, BF16, "mm_out_dw", GW, 512, t)
    du1, du2, dgm = _merge_bwd(dmerged, p, u1, u2)
    dy_hg = _matmul(du1, w_br_hg, NT, F32, "mm_br_hg_dx", tm, HW, GW)
    dy_gla = _matmul(du2, w_br_gla, NT, F32, "mm_br_gla_dx", tm, HW, GW)
    d_w_br_hg = _matmul(y, du1, TN, BF16, "mm_br_hg_dw", HW, GW, t, a_off=0, m_out=HW)
    d_w_br_gla = _matmul(y, du2, TN, BF16, "mm_br_gla_dw", HW, GW, t, a_off=1, m_out=HW)
    dy_hg = send(("w_out", "w_br_hg", "w_br_gla"), (d_w_out, d_w_br_hg, d_w_br_gla), dy_hg)
    do, dgo, sm_post = _post_bwd(dy_hg, dy_gla, o_f, o_b, p, onw)
    dq_f, dk_f, dv_f, dg_f = _scan_bwd(q, k_f, v, g_f, do, st_f, False)
    dq_b, dk_b, dv_b, dg_b = _scan_bwd(q, k_b, v, g_b, do, st_b, True)
    dp, d_lb, d_wgk, d_bgk = _gates_bwd(p, hg_lb, wgk, bgk, dgm, dgo, dq_f, dq_b, dv_f, dv_b, dk_f, dk_b, dg_f, dg_b)
    d_w_in_a = _matmul(h1, dp, TN, BF16, "mm_in_dw_a", 512, 512, t, a_off=0, m_out=D // 2)
    dp = send(("w_in_a",), (d_w_in_a,), dp)
    d_w_in_b = _matmul(h1, dp, TN, BF16, "mm_in_dw_b", 512, 512, t, a_off=1, m_out=D // 2)
    dp = send(("w_in_b",), (d_w_in_b,), dp)
    dh1 = _matmul(dp, w_in, NT, F32, "mm_in_dx", tm, 512, 1024)
    grad_x, sm_pre = _pre_bwd(dh1, dz, z, n_pre1, modc, modx)
    return dict(loss_vec=loss_vec, grad_x=grad_x, sm_final=sm_final, sm_mid=sm_mid, sm_post=sm_post, sm_pre=sm_pre,
                d_lb=d_lb, d_wgk=d_wgk, d_bgk=d_bgk)


MESH = pl.DeviceIdType.MESH
ANY = pl.BlockSpec(memory_space=pl.ANY)
N_REL = N_DEV - 1


def _place():
    return lax.axis_index("x"), lax.axis_index("y"), lax.axis_index("c")


def _slot(p):
    return 4 * p[0] + 2 * p[1] + p[2]


def _all_gather(arrays, name):
    n = len(arrays)

    def body(*refs):
        ins, outs = refs[:n], refs[n:2 * n]
        send_sems, recv_sems, local_sems = refs[2 * n:]
        x, y, c = _place()
        me, sibling = (x, y, c), (x, y, 1 - c)
        chips = [(1 - x, y), (x, 1 - y), (1 - x, 1 - y)]

        def copy(a, k, block, to, src=None):
            dst = outs[a].at[_slot(block)]
            return pltpu.make_async_remote_copy(
                src_ref=dst if src is None else src, dst_ref=dst,
                send_sem=send_sems.at[N_REL * a + k], recv_sem=recv_sems.at[N_REL * a + k],
                device_id=to, device_id_type=MESH)

        mine = [pltpu.make_async_copy(ins[a], outs[a].at[_slot(me)], local_sems.at[a]) for a in range(n)]
        for cp in mine:
            cp.start()
        first = []
        for a in range(n):
            first.append(copy(a, 0, me, sibling, src=ins[a]))
            first += [copy(a, 1 + j, me, (*chip, c), src=ins[a]) for j, chip in enumerate(chips)]
        for cp in first:
            cp.start()
        passed = []
        for j, chip in enumerate(chips):
            for a in range(n):
                copy(a, 1 + j, (*chip, c), me).wait_recv()
                fwd = copy(a, 4 + j, (*chip, c), sibling)
                fwd.start()
                passed.append(fwd)
        for a in range(n):
            copy(a, 0, sibling, me).wait_recv()
        for j, chip in enumerate(chips):
            for a in range(n):
                copy(a, 4 + j, (*chip, 1 - c), me).wait_recv()
        for cp in first + passed:
            cp.wait_send()
        for cp in mine:
            cp.wait()

    return pl.pallas_call(
        body, name=name,
        in_specs=[ANY] * n, out_specs=[ANY] * n,
        out_shape=[jax.ShapeDtypeStruct((N_DEV,) + a.shape, a.dtype) for a in arrays],
        scratch_shapes=[pltpu.SemaphoreType.DMA((N_REL * n,)), pltpu.SemaphoreType.DMA((N_REL * n,)),
                        pltpu.SemaphoreType.DMA((n,))],
    )(*arrays)


def _exchange(arrays, name):
    n = len(arrays)

    def body(*refs):
        ins, outs = refs[:n], refs[n:2 * n]
        send_sems, recv_sems, local_sems = refs[2 * n:]
        x, y, c = _place()
        me = _slot((x, y, c))
        mine = [pltpu.make_async_copy(ins[a].at[me], outs[a].at[me], local_sems.at[a]) for a in range(n)]
        for cp in mine:
            cp.start()
        copies = []
        for a in range(n):
            for k in range(1, N_DEV):
                flip = lambda v, bit: 1 - v if bit else v
                peer = (flip(x, k & 4), flip(y, k & 2), flip(c, k & 1))
                copies.append(pltpu.make_async_remote_copy(
                    src_ref=ins[a].at[_slot(peer)], dst_ref=outs[a].at[me],
                    send_sem=send_sems.at[N_REL * a + k - 1], recv_sem=recv_sems.at[N_REL * a + k - 1],
                    device_id=peer, device_id_type=MESH))
                copies[-1].start()
        i = 0
        for a in range(n):
            for k in range(1, N_DEV):
                flip = lambda v, bit: 1 - v if bit else v
                peer = (flip(x, k & 4), flip(y, k & 2), flip(c, k & 1))
                pltpu.make_async_remote_copy(
                    src_ref=ins[a].at[_slot(peer)], dst_ref=outs[a].at[_slot(peer)],
                    send_sem=send_sems.at[N_REL * a + k - 1], recv_sem=recv_sems.at[N_REL * a + k - 1],
                    device_id=peer, device_id_type=MESH).wait_recv()
                i += 1
        for cp in copies:
            cp.wait_send()
        for cp in mine:
            cp.wait()

    return pl.pallas_call(
        body, name=name,
        in_specs=[ANY] * n, out_specs=[ANY] * n,
        out_shape=[jax.ShapeDtypeStruct(a.shape, a.dtype) for a in arrays],
        scratch_shapes=[pltpu.SemaphoreType.DMA((N_REL * n,)), pltpu.SemaphoreType.DMA((N_REL * n,)),
                        pltpu.SemaphoreType.DMA((n,))],
    )(*arrays)


HBM = pl.BlockSpec(memory_space=pltpu.HBM)
SEM = pl.BlockSpec(memory_space=pltpu.SEMAPHORE)
EFFECT = pltpu.SideEffectType.DATAFLOW_SIDE_EFFECTING


def _peer_of(x, y, c, k):
    flip = lambda v, bit: 1 - v if bit else v
    return flip(x, k & 4), flip(y, k & 2), flip(c, k & 1)


def _view_whole(src, slot):
    return src


def _view_block(src, slot):
    return src.at[slot]


W_IN_SHARD = W_IN_REF // N_DEV


def _view_window(rows):
    def view(src, slot):
        col0 = pl.multiple_of((W_IN_SHARD * slot // DH) * DH, DH)
        return src.at[pl.ds(rows[0], rows[1] - rows[0]), pl.ds(col0, D)]
    return view


def _split_copies(view, srcs, lands, send_sems, recv_sems, local_sems):
    x, y, c = _place()
    me = _slot((x, y, c))
    local, sends, waits = [], [], []
    for a, (src, land) in enumerate(zip(srcs, lands)):
        local.append(pltpu.make_async_copy(view(src, me), land.at[me], local_sems.at[a]))
        for k in range(1, N_DEV):
            peer = _peer_of(x, y, c, k)
            mine = view(src, _slot(peer))
            sems = dict(send_sem=send_sems.at[N_REL * a + k - 1], recv_sem=recv_sems.at[N_REL * a + k - 1],
                        device_id=peer, device_id_type=MESH)
            sends.append(pltpu.make_async_remote_copy(src_ref=mine, dst_ref=land.at[me], **sems))
            waits.append(pltpu.make_async_remote_copy(src_ref=mine, dst_ref=land.at[_slot(peer)], **sems))
    return local, sends, waits


def _split_start(view, land_shapes, srcs, name, after):
    n = len(srcs)
    lands = [lax.empty(shp, s.dtype) for shp, s in zip(land_shapes, srcs)]

    def body(*refs):
        src_refs, land_refs = refs[:n], refs[n:2 * n]
        send_sems, recv_sems, local_sems = refs[2 * n + 1:2 * n + 4]
        token = refs[-1]
        local, sends, _ = _split_copies(view, src_refs, land_refs, send_sems, recv_sems, local_sems)
        for cp in local + sends:
            cp.start()
        token[...] = jnp.zeros_like(token)

    hbm = lambda a: pltpu.with_memory_space_constraint(a, pltpu.HBM)
    out = pl.pallas_call(
        body, name=name,
        out_shape=(pltpu.SemaphoreType.DMA((N_REL * n,)), pltpu.SemaphoreType.DMA((N_REL * n,)),
                   pltpu.SemaphoreType.DMA((n,)),
                   *[pltpu.HBM(s.shape, s.dtype) for s in srcs], *[pltpu.HBM(l.shape, l.dtype) for l in lands],
                   jax.ShapeDtypeStruct((8, DH), F32)),
        in_specs=[HBM] * (2 * n) + [ANY],
        out_specs=(SEM, SEM, SEM, *([HBM] * (2 * n)), pl.BlockSpec(memory_space=pltpu.VMEM)),
        input_output_aliases={i: 3 + i for i in range(2 * n)},
        compiler_params=pltpu.CompilerParams(has_side_effects=EFFECT),
    )(*[hbm(s) for s in srcs], *[hbm(l) for l in lands], after)
    handle = dict(view=view, n=n, sems=out[:3], srcs=list(out[3:3 + n]), lands=list(out[3 + n:3 + 2 * n]))
    return handle, out[-1]


def _split_wait(handle, name, after, srcs=None):
    view, n, sems, lands = handle["view"], handle["n"], handle["sems"], handle["lands"]
    srcs = handle["srcs"] if srcs is None else srcs
    afters = list(after) if isinstance(after, (list, tuple)) else [after]

    def body(*refs):
        src_refs, land_refs = refs[:n], refs[n:2 * n]
        send_sems, recv_sems, local_sems = refs[2 * n:2 * n + 3]
        local, _, waits = _split_copies(view, src_refs, land_refs, send_sems, recv_sems, local_sems)
        for cp in waits:
            cp.wait_send()
            cp.wait_recv()
        for cp in local:
            cp.wait()

    out = pl.pallas_call(
        body, name=name,
        out_shape=(*[pltpu.HBM(s.shape, s.dtype) for s in srcs], *[pltpu.HBM(l.shape, l.dtype) for l in lands]),
        in_specs=[HBM] * (2 * n) + [SEM, SEM, SEM] + [ANY] * len(afters),
        out_specs=tuple([HBM] * (2 * n)),
        input_output_aliases={i: i for i in range(2 * n)},
        compiler_params=pltpu.CompilerParams(has_side_effects=EFFECT),
    )(*srcs, *lands, *sems, *afters)
    handle["srcs"] = list(out[:n])
    return list(out[n:])


def _tie(x, token, name):
    def body(x_ref, t_ref, o_ref):
        pass

    return pl.pallas_call(
        body, name=name, out_shape=jax.ShapeDtypeStruct(x.shape, x.dtype),
        in_specs=[ANY, ANY], out_specs=ANY, input_output_aliases={0: 0},
    )(x, token)


def _mod_fwd(a, w, b):
    def body(a_ref, w_ref, b_ref, o_ref):
        o_ref[...] = _dot(_silu(a_ref[...]), w_ref[...], NN, precision=HI) + b_ref[...]

    return pl.pallas_call(
        body, name="mod_fwd", out_shape=jax.ShapeDtypeStruct((a.shape[0], w.shape[1]), F32),
        compiler_params=pltpu.CompilerParams(vmem_limit_bytes=VMEM_LIMIT),
    )(a, w, b)


def _mod_bwd(a, d, w):
    def body(a_ref, d_ref, w_ref, dw_ref, dc_ref):
        av = a_ref[...]
        dv = d_ref[...]
        dw_ref[...] = _dot(_silu(av), dv, TN, precision=HI)
        da = _dot(dv[0:8, :], w_ref[...], NT, precision=HI) * _dsilu(av[0:8, :])
        row = lax.broadcasted_iota(jnp.int32, da.shape, 0)
        dc_ref[...] = jnp.where(row == 0, da, 0.0)

    return pl.pallas_call(
        body, name="mod_bwd",
        out_shape=[jax.ShapeDtypeStruct(w.shape, F32), jax.ShapeDtypeStruct((8, w.shape[0]), F32)],
        compiler_params=pltpu.CompilerParams(vmem_limit_bytes=VMEM_LIMIT),
    )(a, d, w)


def _sum_devices(g):
    def body(g_ref, o_ref):
        acc = g_ref[0]
        for i in range(1, g.shape[0]):
            acc = acc + g_ref[i]
        o_ref[...] = acc

    return pl.pallas_call(body, name="sum_devices_%d" % g.shape[1],
                          out_shape=jax.ShapeDtypeStruct(g.shape[1:], F32))(g)


def _sum_windows(g, name):
    n, r, c = g.shape
    tr = 128

    def body(g_ref, o_ref):
        x, y, cc = _place()
        lane0 = (W_IN_SHARD * _slot((x, y, cc))) % DH
        acc = g_ref[0].astype(F32)
        for i in range(1, n):
            acc = acc + g_ref[i].astype(F32)
        o_ref[...] = pltpu.roll(acc, (c - lane0) % c, 1).T

    return pl.pallas_call(
        body, name=name, grid=(r // tr,),
        in_specs=[pl.BlockSpec((n, tr, c), lambda i: (0, i, 0))],
        out_specs=pl.BlockSpec((c, tr), lambda i: (0, i)),
        out_shape=jax.ShapeDtypeStruct((c, r), F32),
        compiler_params=_cp("parallel"),
    )(g)


def _adam_rows(r, c, n):
    budget = 6 * 1024 * 1024
    best = None
    for tr in range(16, r + 1, 16):
        if r % tr == 0 and tr * c * (2 * n + 28) <= budget:
            best = tr
    return best if best is not None else r


def _adamw(g, w, m, v, name):
    n, r, c = g.shape
    tr = _adam_rows(r, c, n)
    bc1 = 1.0 - ADAM_B1 ** ADAM_STEP
    bc2 = 1.0 - ADAM_B2 ** ADAM_STEP

    def body(g_ref, w_ref, m_ref, v_ref, go_ref, d_ref, mo_ref, vo_ref):
        grad = g_ref[0].astype(F32)
        for i in range(1, n):
            grad = grad + g_ref[i].astype(F32)
        go_ref[...] = grad
        m_new = ADAM_B1 * m_ref[...] + (1.0 - ADAM_B1) * grad
        v_new = ADAM_B2 * v_ref[...] + (1.0 - ADAM_B2) * (grad * grad)
        mo_ref[...] = m_new
        vo_ref[...] = v_new
        d_ref[...] = -ADAM_LR * ((m_new / bc1) / (jnp.sqrt(v_new / bc2) + ADAM_EPS) + ADAM_WD * w_ref[...])

    blk = pl.BlockSpec((tr, c), lambda i: (i, 0))
    out = jax.ShapeDtypeStruct((r, c), F32)
    return pl.pallas_call(
        body, name=name, grid=(r // tr,),
        in_specs=[pl.BlockSpec((n, tr, c), lambda i: (0, i, 0)), blk, blk, blk],
        out_specs=[blk] * 4, out_shape=[out] * 4,
        compiler_params=_cp("parallel"),
    )(g, w, m, v)


def kernel(x, c, ctx, c_ctx, w_mod, b_mod, norm_pre1, norm_post1, norm_pre2, norm_post2, w_in, hg_lb, hg_onorm, gla_w_gk, gla_b_gk, gla_onorm, w_br_hg, w_br_gla, w_out, w_ff_gate, w_ff_up, w_ff_down, loss_target, m_c_ctx, m_w_mod, m_b_mod, m_norm_pre1, m_norm_post1, m_norm_pre2, m_norm_post2, m_w_in, m_hg_lb, m_hg_onorm, m_gla_w_gk, m_gla_b_gk, m_gla_onorm, m_w_br_hg, m_w_br_gla, m_w_out, m_w_ff_gate, m_w_ff_up, m_w_ff_down, v_c_ctx, v_w_mod, v_b_mod, v_norm_pre1, v_norm_post1, v_norm_pre2, v_norm_post2, v_w_in, v_hg_lb, v_hg_onorm, v_gla_w_gk, v_gla_b_gk, v_gla_onorm, v_w_br_hg, v_w_br_gla, v_w_out, v_w_ff_gate, v_w_ff_up, v_w_ff_down):
    xi, yi, ci = lax.axis_index("x"), lax.axis_index("y"), lax.axis_index("c")
    me = 4 * xi + 2 * yi + ci
    t = CTX + x.shape[1]

    c_all, lb_g, wgk_g, bgk_g = _all_gather([c, hg_lb, gla_w_gk[0], gla_b_gk[0]], "ag_small")
    tr_ = lambda a: jnp.swapaxes(a[0], 0, 1)
    big = [w_in[0], w_br_hg[0], w_br_gla[0], w_out[0], tr_(w_ff_gate), tr_(w_ff_up), w_ff_down[0]]
    big_bf = [w.astype(BF16) for w in big]
    g_in, = _all_gather(big_bf[:1], "ag_w_in")
    gathered = lambda arrs: [(N_DEV,) + a.shape for a in arrs]
    mix_handle, tok = _split_start(_view_whole, gathered(big_bf[1:4]), big_bf[1:4], "ag_mix_start", g_in)
    ffn_handle, tok = _split_start(_view_whole, gathered(big_bf[4:]), big_bf[4:], "ag_ffn_start", tok)
    cols = lambda g: jnp.transpose(g, (1, 0, 2)).reshape(g.shape[1], N_DEV * g.shape[2])
    w_in_k = _tie(_layout_w_in(cols(g_in)), tok, "tie_w_in")

    def get_mix(after):
        g_brh, g_brg, g_out = _split_wait(mix_handle, "ag_mix_wait", after)
        return _gate_cols(cols(g_brh)), _gate_cols(cols(g_brg)), _gate_rows(g_out.reshape(D, D))

    def get_ffn(after):
        g_gate, g_up, g_down = _split_wait(ffn_handle, "ag_ffn_wait", after)
        return (g_gate.reshape(D_FF, D), g_up.reshape(D_FF, D)), g_down.reshape(D_FF, D)

    hg_lb_full = jnp.transpose(lb_g, (1, 2, 0, 3)).reshape(2, 2, HW)
    wgk_k = _layout_wgk(jnp.transpose(wgk_g, (1, 2, 0, 3)).reshape(2, 16, HW)).astype(BF16)
    bgk_k = jnp.transpose(bgk_g, (1, 0, 2)).reshape(1, D)
    onw = jnp.concatenate([jnp.tile(hg_onorm, (1, NH // 2)), jnp.tile(gla_onorm, (1, NH // 2))], axis=1)

    n_mod = w_mod.shape[2]
    a9 = jnp.concatenate([c_ctx[None], c_all[:, 0], jnp.zeros((16 - 1 - N_DEV, D), F32)], axis=0)
    b_loc = lax.dynamic_slice(b_mod, (0, me * n_mod), (1, n_mod))
    s_loc = _mod_fwd(a9, w_mod[0], b_loc)
    s_all, = _all_gather([s_loc], "ag_mod")
    mod_all = jnp.transpose(s_all, (1, 0, 2)).reshape(16, N_DEV * n_mod)
    pad8 = lambda m: jnp.concatenate([m.reshape(6, D), jnp.zeros((2, D), F32)], axis=0)
    modc = pad8(mod_all[0])
    modx = pad8(lax.dynamic_slice(mod_all, (1 + me, 0), (1, N_DEV * n_mod))[0])

    z = jnp.concatenate([ctx[0], x[0]], axis=0)
    norms = (norm_pre1, norm_post1, norm_pre2, norm_post2)
    shard = lambda d: jnp.transpose(d.reshape(d.shape[0], N_DEV, -1), (1, 0, 2)).astype(BF16)
    rowshard = lambda d: d.reshape(N_DEV, d.shape[0] // N_DEV, d.shape[1]).astype(BF16)
    sent, w_in_grad = [], {}

    def send_w_in(i, x_after):
        half, rows = W_IN_GRAD_CHUNKS[i]
        handle, tok = _split_start(_view_window(rows), [(N_DEV, rows[1] - rows[0], D)], w_in_grad[half],
                                   "grads_w_in%d_start" % i, x_after)
        w_in_grad[half] = handle["srcs"]
        sent.append(("w_in%d" % i, ["w_in#%d" % i], handle))
        return _tie(x_after, tok, "tie_w_in%d" % i)

    def send(names, grads, x_after):
        if names == ("w_in_a",):
            w_in_grad["a"] = list(grads)
            return send_w_in(0, x_after)
        if names == ("w_in_b",):
            w_in_grad["b"] = list(grads)
            return x_after
        arrs, leaves = [], []
        for nm, g in zip(names, grads):
            if nm in ("w_gate_t", "w_up_t"):
                arrs.append(rowshard(g))
                leaves.append({"w_gate_t": "w_ff_gate", "w_up_t": "w_ff_up"}[nm])
            elif nm == "w_down":
                arrs.append(rowshard(g))
                leaves.append("w_ff_down")
            elif nm == "w_out":
                arrs.append(rowshard(g[GOFF:GOFF + D]))
                leaves.append(nm)
            else:
                arrs.append(shard(g[:, GOFF:GOFF + D]))
                leaves.append(nm)
        handle, tok = _split_start(_view_block, [a.shape for a in arrs], arrs, "grads_%s_start" % names[0], x_after)
        sent.append((names[0], leaves, handle))
        return _tie(x_after, tok, "tie_" + names[0])

    r = _local_step(z, loss_target[0], modc, modx, norms, onw, hg_lb_full, wgk_k, bgk_k,
                    w_in_k, get_mix, get_ffn, send)
    grad_x = r["grad_x"][None]

    sm_pre, sm_mid, sm_fin = r["sm_pre"], r["sm_mid"], r["sm_final"]
    dmodc = jnp.stack([sm_pre[0], sm_pre[2], sm_mid[4], sm_mid[0], sm_mid[2], sm_fin[0]]).reshape(-1)
    dmodx = jnp.stack([sm_pre[1], sm_pre[3], sm_mid[5], sm_mid[1], sm_mid[3], sm_fin[1]]).reshape(-1)
    on = r["sm_post"][0].reshape(NH, DH)
    pieces = [dmodc, dmodx, sm_pre[4], sm_mid[7], sm_mid[6], sm_fin[2], on[:NH // 2].sum(0), on[NH // 2:].sum(0),
              r["d_lb"][:2].reshape(-1), _unlayout_wgk(r["d_wgk"]).reshape(-1), r["d_bgk"][0]]
    loss_local = (0.5 / D) * jnp.sum(r["loss_vec"])
    pieces.append(jnp.concatenate([loss_local.reshape(1), jnp.zeros((DH - 1,), F32)]))
    sizes = [p.shape[0] for p in pieces]
    pack = jnp.concatenate(pieces).reshape(-1, DH)
    pack_all, = _all_gather([pack], "ag_small_grads")
    pack_all = send_w_in(1, pack_all)
    tot = _sum_devices(pack_all).reshape(-1)
    offs = [sum(sizes[:i]) for i in range(len(sizes))]
    part = lambda i: tot[offs[i]:offs[i] + sizes[i]]
    dmodc_t, dmodx_t = part(0), part(1)
    g_b_mod = (dmodc_t + dmodx_t)[None]
    g_norms = [part(i)[None] for i in (2, 3, 4, 5)]
    g_hg_on, g_gla_on = part(6)[None], part(7)[None]
    lb0 = lax.dynamic_slice(part(8).reshape(2, HW), (0, me * (HW // N_DEV)), (2, HW // N_DEV))
    g_hg_lb = jnp.stack([lb0, -lb0])
    g_wgk = lax.dynamic_slice(part(9).reshape(2, 16, HW), (0, 0, me * (HW // N_DEV)), (2, 16, HW // N_DEV))[None]
    g_bgk = lax.dynamic_slice(part(10).reshape(2, HW), (0, me * (HW // N_DEV)), (2, HW // N_DEV))[None]
    loss = part(11)[0]

    dmx_all = pack_all.reshape(N_DEV, -1)[:, sizes[0]:sizes[0] + sizes[1]]
    d9 = jnp.concatenate([lax.dynamic_slice(dmodc_t[None], (0, me * n_mod), (1, n_mod)),
                          lax.dynamic_slice(dmx_all, (0, me * n_mod), (N_DEV, n_mod)),
                          jnp.zeros((16 - 1 - N_DEV, n_mod), F32)], axis=0)
    g_w_mod, dcc_part = _mod_bwd(a9, d9, w_mod[0])
    dcc_all, = _all_gather([dcc_part], "ag_c_ctx")
    dcc_all = send_w_in(2, dcc_all)
    g_c_ctx = _sum_devices(dcc_all)[0]

    recv = {}
    for first, leaves, handle in sent:
        if not first.startswith("w_in"):
            recv.update(zip(leaves, _split_wait(handle, "grads_%s_wait" % first, g_c_ctx)))
    moms = [(m_w_in, v_w_in), (m_w_br_hg, v_w_br_hg), (m_w_br_gla, v_w_br_gla), (m_w_out, v_w_out),
            (m_w_ff_gate, v_w_ff_gate), (m_w_ff_up, v_w_ff_up), (m_w_ff_down, v_w_ff_down)]
    names = ["w_in", "w_br_hg", "w_br_gla", "w_out", "w_ff_gate", "w_ff_up", "w_ff_down"]
    res = {}

    def update(nm, w, m, v):
        if nm in ("w_ff_gate", "w_ff_up"):
            outs = _adamw(recv[nm], w, tr_(m), tr_(v), "adamw_" + nm)
            res[nm] = [jnp.swapaxes(o, 0, 1)[None] for o in outs]
        else:
            res[nm] = [o[None] for o in _adamw(recv[nm], w, m[0], v[0], "adamw_" + nm)]

    for nm, w, (m, v) in list(zip(names, big, moms))[1:]:
        update(nm, w, m, v)
    res["w_mod"] = [o[None] for o in _adamw(g_w_mod[None], w_mod[0], m_w_mod[0], v_w_mod[0], "adamw_w_mod")]

    small = [("c_ctx", c_ctx, m_c_ctx, v_c_ctx, g_c_ctx), ("b_mod", b_mod, m_b_mod, v_b_mod, g_b_mod),
             ("norm_pre1", norm_pre1, m_norm_pre1, v_norm_pre1, g_norms[0]),
             ("norm_post1", norm_post1, m_norm_post1, v_norm_post1, g_norms[1]),
             ("norm_pre2", norm_pre2, m_norm_pre2, v_norm_pre2, g_norms[2]),
             ("norm_post2", norm_post2, m_norm_post2, v_norm_post2, g_norms[3]),
             ("hg_lb", hg_lb, m_hg_lb, v_hg_lb, g_hg_lb), ("hg_onorm", hg_onorm, m_hg_onorm, v_hg_onorm, g_hg_on),
             ("gla_w_gk", gla_w_gk, m_gla_w_gk, v_gla_w_gk, g_wgk), ("gla_b_gk", gla_b_gk, m_gla_b_gk, v_gla_b_gk, g_bgk),
             ("gla_onorm", gla_onorm, m_gla_onorm, v_gla_onorm, g_gla_on)]
    flat = lambda k: jnp.concatenate([s[k].reshape(-1) for s in small]).reshape(-1, DH)
    outs = _adamw(flat(4)[None], flat(1), flat(2), flat(3), "adamw_small")
    off = 0
    for nm, w, _, _, _ in small:
        res[nm] = [o.reshape(-1)[off:off + w.size].reshape(w.shape) for o in outs]
        off += w.size

    done = [res[nm][0] for nm in names[1:]] + [res["w_mod"][0], outs[0]]
    sums = []
    for i, (first, leaves, handle) in enumerate(s for s in sent if s[0].startswith("w_in")):
        half = W_IN_GRAD_CHUNKS[i][0]
        land, = _split_wait(handle, "grads_%s_wait" % first, done, srcs=w_in_grad[half])
        w_in_grad[half] = handle["srcs"]
        sums.append(_sum_windows(land, "sum_windows%d" % i))
    g_t = jnp.concatenate(sums, axis=1)[:W_IN_SHARD]
    lin = lambda a: a.reshape(W_IN_SHARD * D // DH, DH)
    outs = _adamw(lin(g_t)[None], lin(tr_(w_in)), lin(tr_(m_w_in)), lin(tr_(v_w_in)), "adamw_w_in")
    res["w_in"] = [jnp.swapaxes(o.reshape(W_IN_SHARD, D), 0, 1)[None] for o in outs]

    order = ["c_ctx", "w_mod", "b_mod", "norm_pre1", "norm_post1", "norm_pre2", "norm_post2", "w_in", "hg_lb",
             "hg_onorm", "gla_w_gk", "gla_b_gk", "gla_onorm", "w_br_hg", "w_br_gla", "w_out", "w_ff_gate", "w_ff_up",
             "w_ff_down"]
    return (loss, grad_x, *[res[n][k] for k in range(4) for n in order])
```

```python
import functools

import jax
import jax.numpy as jnp
from jax import lax
from jax.experimental import pallas as pl
from jax.experimental.pallas import tpu as pltpu

F32 = jnp.float32
BF16 = jnp.bfloat16
HI = lax.Precision.HIGHEST

N_DEV = 8
D = 1024
CTX = 256
HW = 512
DH = 128
NH = 8
D_FF = 2816
EPS = 1e-6
GLA_NORM = 16.0
CHUNK = 64
TR = 256
NCT = CTX // TR
W_IN_COLS = 7168
MAIN0 = 0
LR0 = 4608
GW = 1152
GOFF = 32
GATE_HG0 = LR0
GATE_GLA0 = LR0 + D
LEVELS = (32, 16, 8)
EXP_CLAMP = 80.0
VMEM_LIMIT = 48 * 1024 * 1024

ADAM_LR, ADAM_B1, ADAM_B2, ADAM_EPS, ADAM_WD, ADAM_STEP = 0.001, 0.9, 0.999, 1e-08, 0.01, 10


def _cp(*sem):
    return pltpu.CompilerParams(dimension_semantics=sem, vmem_limit_bytes=VMEM_LIMIT)


def _sig(x):
    return jax.nn.sigmoid(x)


def _silu(x):
    return x * _sig(x)


def _dsilu(x):
    s = _sig(x)
    return s * (1.0 + x * (1.0 - s))


def _rstd(x):
    return lax.rsqrt(jnp.mean(x * x, axis=-1, keepdims=True) + EPS)


def _rms_bwd(a, y, r):
    return r * (a - y * (r * r) * jnp.mean(a * y, axis=-1, keepdims=True))


def _colsum(x):
    return jnp.sum(x, axis=0, keepdims=True)


def _dot(a, b, dims, precision=None):
    return lax.dot_general(a, b, (dims, ((), ())), preferred_element_type=F32, precision=precision)


NN = ((1,), (0,))
NT = ((1,), (1,))
TN = ((0,), (0,))

SCAN_HEADS_FWD = 4
SCAN_HEADS_BWD = 4


def _split_dot(m, x):
    mb = m.astype(BF16)
    x1 = x.astype(BF16)
    r1 = x - x1.astype(F32)
    x2 = r1.astype(BF16)
    x3 = (r1 - x2.astype(F32)).astype(BF16)
    return _dot(mb, x1, NN) + _dot(mb, x2, NN) + _dot(mb, x3, NN)


def _matmul(a, b, dims, out_dtype, name, tm, tn, tk, a_off=0, m_out=None):
    pair = isinstance(b, (tuple, list))
    bs = list(b) if pair else [b]
    b1 = bs[0]
    rows = b1.shape[0] * len(bs)
    half = None
    if dims == NN:
        m, k, n = a.shape[0], rows, b1.shape[1]
        a_spec = pl.BlockSpec((tm, tk), lambda i, j, kk: (i, kk + a_off))
        half = b1.shape[0] // tk
        b_maps = [lambda i, j, kk: (kk, j)] if not pair else [
            lambda i, j, kk: (jnp.minimum(kk, half - 1), j), lambda i, j, kk: (jnp.maximum(kk - half, 0), j)]
        b_specs = [pl.BlockSpec((tk, tn), f) for f in b_maps]
        axis = 2
    elif dims == NT:
        m, k, n = a.shape[0], b1.shape[1], rows
        a_spec = pl.BlockSpec((tm, tk), lambda i, j, kk: (i, kk + a_off))
        half = b1.shape[0] // tn
        b_maps = [lambda i, j, kk: (j, kk)] if not pair else [
            lambda i, j, kk: (jnp.minimum(j, half - 1), kk), lambda i, j, kk: (jnp.maximum(j - half, 0), kk)]
        b_specs = [pl.BlockSpec((tn, tk), f) for f in b_maps]
        axis = 1
    else:
        assert not pair
        m, k = (a.shape[1] if m_out is None else m_out), a.shape[0]
        n = b1.shape[1]
        a_spec = pl.BlockSpec((tk, tm), lambda i, j, kk: (kk, i + a_off))
        b_specs = [pl.BlockSpec((tk, tn), lambda i, j, kk: (kk, j))]
    assert m % tm == 0 and n % tn == 0 and k % tk == 0, (name, m, n, k, tm, tn, tk)
    nk = k // tk
    nb = len(bs)

    def body(a_ref, *refs):
        o_ref = refs[nb]
        if pair:
            bv = jnp.where(pl.program_id(axis) < half, refs[0][...], refs[1][...])
        else:
            bv = refs[0][...]
        part = _dot(a_ref[...], bv, dims)
        if nk == 1:
            o_ref[...] = part.astype(o_ref.dtype)
            return
        acc_ref = refs[nb + 1]
        kk = pl.program_id(2)

        @pl.when(kk == 0)
        def _():
            acc_ref[...] = part

        @pl.when(kk > 0)
        def _():
            acc_ref[...] += part

        @pl.when(kk == nk - 1)
        def _():
            o_ref[...] = acc_ref[...].astype(o_ref.dtype)

    return pl.pallas_call(
        body,
        name=name,
        grid=(m // tm, n // tn, nk),
        in_specs=[a_spec] + b_specs,
        out_specs=pl.BlockSpec((tm, tn), lambda i, j, kk: (i, j)),
        out_shape=jax.ShapeDtypeStruct((m, n), out_dtype),
        scratch_shapes=[] if nk == 1 else [pltpu.VMEM((tm, tn), F32)],
        compiler_params=_cp("parallel", "parallel", "arbitrary"),
    )(a, *bs)


def _row(c):
    return pl.BlockSpec((TR, c), lambda i: (i, 0))


def _rowcol(width, cb):
    return pl.BlockSpec((TR, width), lambda i: (i, cb))


def _full(shape):
    return pl.BlockSpec(shape, lambda i: (0,) * len(shape))


def _mod_row(mc_ref, mx_ref, k, is_ctx):
    return jnp.where(is_ctx, mc_ref[k:k + 1, :], mx_ref[k:k + 1, :])


def _acc_row(ref, k, val):
    ref[k:k + 1, :] += val


def _acc_mod(ref, k, is_ctx, val):
    zero = jnp.zeros_like(val)
    ref[k:k + 1, :] += jnp.where(is_ctx, val, zero)
    ref[k + 1:k + 2, :] += jnp.where(is_ctx, zero, val)


def _prenorm(z, nw, modc, modx, i_shift, i_scale, name):
    t = z.shape[0]

    def body(z_ref, nw_ref, mc_ref, mx_ref, h_ref):
        is_ctx = pl.program_id(0) < NCT
        x = z_ref[...]
        n = x * _rstd(x) * nw_ref[...]
        h = n * (1.0 + _mod_row(mc_ref, mx_ref, i_scale, is_ctx)) + _mod_row(mc_ref, mx_ref, i_shift, is_ctx)
        h_ref[...] = h.astype(BF16)

    return pl.pallas_call(
        body, name=name, grid=(t // TR,),
        in_specs=[_row(D), _full((1, D)), _full((8, D)), _full((8, D))],
        out_specs=_row(D),
        out_shape=jax.ShapeDtypeStruct((t, D), BF16),
        compiler_params=_cp("parallel"),
    )(z, nw, modc, modx)


def _hg_lb(lb_ref, d):
    a0 = lb_ref[0, d:d + 1, :]
    a1 = lb_ref[1, d:d + 1, :]
    mx = jnp.maximum(a0, a1)
    e0 = jnp.exp(a0 - mx)
    e1 = jnp.exp(a1 - mx)
    return e0 / (e0 + e1)


def _log_sigmoid(x):
    return jnp.minimum(x, 0.0) - jnp.log(1.0 + jnp.exp(-jnp.abs(x)))


def _gates_fwd(p, hg_lb, wgk, bgk):
    t = p.shape[0]
    seg = lambda j: _rowcol(HW, MAIN0 // HW + j)

    def body(hq_ref, hi_ref, hf_ref, hb_ref, gq_ref, gk_ref, gv_ref, lr_ref, lb_ref, wgk_ref, bgk_ref,
             q_ref, v_ref, kf_ref, kb_ref, gf_ref, gb_ref):
        q_ref[:, :HW] = _silu(hq_ref[...].astype(F32)).astype(BF16)
        q_ref[:, HW:] = (gq_ref[...].astype(F32) * (DH ** -0.5)).astype(BF16)
        v_ref[:, :HW] = hi_ref[...]
        v_ref[:, HW:] = gv_ref[...]
        xg = _dot(lr_ref[...].astype(BF16), wgk_ref[...], NN) + bgk_ref[...]
        for d, (raw_ref, k_ref, g_ref) in enumerate(((hf_ref, kf_ref, gf_ref), (hb_ref, kb_ref, gb_ref))):
            lbd = _hg_lb(lb_ref, d)
            f = lbd + (1.0 - lbd) * _sig(raw_ref[...].astype(F32))
            k_ref[:, :HW] = (1.0 - f).astype(BF16)
            k_ref[:, HW:] = gk_ref[...]
            g_ref[:, :HW] = jnp.log(f)
            g_ref[:, HW:] = _log_sigmoid(xg[:, d * HW:(d + 1) * HW]) * (1.0 / GLA_NORM)

    out = jax.ShapeDtypeStruct((t, D), F32)
    outb = jax.ShapeDtypeStruct((t, D), BF16)
    return pl.pallas_call(
        body, name="gates_fwd", grid=(t // TR,),
        in_specs=[seg(0), seg(1), seg(2), seg(3), seg(5), seg(6), seg(7), _rowcol(DH, LR0 // DH),
                  _full((2, 2, HW)), _full((DH, D)), _full((1, D))],
        out_specs=[_row(D)] * 6,
        out_shape=[outb] * 4 + [out] * 2,
        compiler_params=_cp("parallel"),
    )(p, p, p, p, p, p, p, p, hg_lb, wgk, bgk)


def _post_fwd(o_fw, o_bw, p, onw):
    t = o_fw.shape[0]

    def body(of_ref, ob_ref, g1_ref, g2_ref, w_ref, y_ref):
        for h in range(NH):
            sl = slice(h * DH, (h + 1) * DH)
            o = of_ref[:, sl] + ob_ref[:, sl]
            g_ref = g1_ref if h < NH // 2 else g2_ref
            gs = slice((h % (NH // 2)) * DH, (h % (NH // 2) + 1) * DH)
            n = o * _rstd(o) * w_ref[:, sl]
            y_ref[:, sl] = (n * _silu(g_ref[:, gs].astype(F32))).astype(BF16)

    return pl.pallas_call(
        body, name="post_fwd", grid=(t // TR,),
        in_specs=[_row(D), _row(D), _rowcol(HW, MAIN0 // HW + 4), _rowcol(HW, MAIN0 // HW + 8), _full((1, D))],
        out_specs=_row(D),
        out_shape=jax.ShapeDtypeStruct((t, D), BF16),
        compiler_params=_cp("parallel"),
    )(o_fw, o_bw, p, p, onw)


def _gate_window_specs(col0):
    return [_rowcol(HW, col0 // HW), _rowcol(HW, col0 // HW + 1), _rowcol(DH, (col0 + 2 * HW) // DH)]


def _gate_window(refs):
    return jnp.concatenate([r[...].astype(F32) for r in refs], axis=1)


def _merge_fwd(p, u1, u2):
    t = p.shape[0]

    def body(a0, a1, a2, b0, b1, b2, u1_ref, u2_ref, m_ref):
        f = lambda r: r[...].astype(F32)
        m_ref[...] = (_sig(_gate_window((a0, a1, a2))) * f(u1_ref)
                      + _sig(_gate_window((b0, b1, b2))) * f(u2_ref)).astype(BF16)

    return pl.pallas_call(
        body, name="merge_fwd", grid=(t // TR,),
        in_specs=_gate_window_specs(GATE_HG0) + _gate_window_specs(GATE_GLA0) + [_row(GW), _row(GW)],
        out_specs=_row(GW),
        out_shape=jax.ShapeDtypeStruct((t, GW), BF16),
        compiler_params=_cp("parallel"),
    )(p, p, p, p, p, p, u1, u2)


def _mid_fwd(z, y1, nw_post, nw_pre, modc, modx):
    t = z.shape[0]

    def body(z_ref, y_ref, wpo_ref, wpr_ref, mc_ref, mx_ref, z1_ref, h_ref):
        is_ctx = pl.program_id(0) < NCT
        y = y_ref[...].astype(F32)
        z1 = z_ref[...] + _mod_row(mc_ref, mx_ref, 2, is_ctx) * (y * _rstd(y) * wpo_ref[...])
        z1_ref[...] = z1
        n = z1 * _rstd(z1) * wpr_ref[...]
        h = n * (1.0 + _mod_row(mc_ref, mx_ref, 4, is_ctx)) + _mod_row(mc_ref, mx_ref, 3, is_ctx)
        h_ref[...] = h.astype(BF16)

    return pl.pallas_call(
        body, name="mid_fwd", grid=(t // TR,),
        in_specs=[_row(D), _row(D), _full((1, D)), _full((1, D)), _full((8, D)), _full((8, D))],
        out_specs=[_row(D), _row(D)],
        out_shape=[jax.ShapeDtypeStruct((t, D), F32), jax.ShapeDtypeStruct((t, D), BF16)],
        compiler_params=_cp("parallel"),
    )(z, y1, nw_post, nw_pre, modc, modx)


def _swiglu_fwd(uv):
    t = uv.shape[0]

    def body(u_ref, v_ref, a_ref):
        a_ref[...] = (_silu(u_ref[...].astype(F32)) * v_ref[...].astype(F32)).astype(BF16)

    return pl.pallas_call(
        body, name="swiglu_fwd", grid=(t // TR,),
        in_specs=[_rowcol(D_FF, 0), _rowcol(D_FF, 1)],
        out_specs=_row(D_FF),
        out_shape=jax.ShapeDtypeStruct((t, D_FF), BF16),
        compiler_params=_cp("parallel"),
    )(uv, uv)


def _swiglu_bwd(uv, da):
    t = uv.shape[0]

    def body(u_ref, v_ref, da_ref, d_ref):
        u = u_ref[...].astype(F32)
        d = da_ref[...].astype(F32)
        d_ref[:, :D_FF] = (d * v_ref[...].astype(F32) * _dsilu(u)).astype(BF16)
        d_ref[:, D_FF:] = (d * _silu(u)).astype(BF16)

    return pl.pallas_call(
        body, name="swiglu_bwd", grid=(t // TR,),
        in_specs=[_rowcol(D_FF, 0), _rowcol(D_FF, 1), _row(D_FF)],
        out_specs=_row(2 * D_FF),
        out_shape=jax.ShapeDtypeStruct((t, 2 * D_FF), BF16),
        compiler_params=_cp("parallel"),
    )(uv, uv, da)


def _final(z1, y2, target, nw, modc, modx):
    t = z1.shape[0]

    def body(z1_ref, y_ref, tg_ref, w_ref, mc_ref, mx_ref, dz_ref, dy_ref, loss_ref, sm_ref):
        i = pl.program_id(0)
        is_ctx = i < NCT

        @pl.when(i == 0)
        def _():
            loss_ref[...] = jnp.zeros_like(loss_ref)
            sm_ref[...] = jnp.zeros_like(sm_ref)

        g = _mod_row(mc_ref, mx_ref, 5, is_ctx)
        y = y_ref[...].astype(F32)
        r = _rstd(y)
        w = w_ref[...]
        yr = y * r
        n = yr * w
        e = z1_ref[...] + g * n - tg_ref[...]
        lat = jnp.where(is_ctx, 0.0, 1.0)
        loss_ref[...] += lat * _colsum(e * e)
        dz = e * (lat / D)
        dz_ref[...] = dz
        _acc_mod(sm_ref, 0, is_ctx, _colsum(dz * n))
        dn = dz * g
        _acc_row(sm_ref, 2, _colsum(dn * yr))
        dy_ref[...] = _rms_bwd(dn * w, y, r).astype(BF16)

    return pl.pallas_call(
        body, name="final", grid=(t // TR,),
        in_specs=[_row(D), _row(D), pl.BlockSpec((TR, D), lambda i: (jnp.maximum(i - NCT, 0), 0)),
                  _full((1, D)), _full((8, D)), _full((8, D))],
        out_specs=[_row(D), _row(D), _full((1, D)), _full((8, D))],
        out_shape=[jax.ShapeDtypeStruct((t, D), F32), jax.ShapeDtypeStruct((t, D), BF16),
                   jax.ShapeDtypeStruct((1, D), F32), jax.ShapeDtypeStruct((8, D), F32)],
        compiler_params=_cp("arbitrary"),
    )(z1, y2, target, nw, modc, modx)


def _mid_bwd(dh2, dz, z, z1, y1, nw_post, nw_pre, modc, modx):
    t = z.shape[0]

    def body(dh_ref, dz_ref, z_ref, z1_ref, y_ref, wpo_ref, wpr_ref, mc_ref, mx_ref, dzo_ref, dy_ref, sm_ref):
        i = pl.program_id(0)
        is_ctx = i < NCT

        @pl.when(i == 0)
        def _():
            sm_ref[...] = jnp.zeros_like(sm_ref)

        dh = dh_ref[...].astype(F32)
        z1 = z1_ref[...]
        r = _rstd(z1)
        zr = z1 * r
        wpr = wpr_ref[...]
        n = zr * wpr
        _acc_mod(sm_ref, 0, is_ctx, _colsum(dh))
        _acc_mod(sm_ref, 2, is_ctx, _colsum(dh * n))
        dn = dh * (1.0 + _mod_row(mc_ref, mx_ref, 4, is_ctx))
        _acc_row(sm_ref, 6, _colsum(dn * zr))
        dz1 = dz_ref[...] + _rms_bwd(dn * wpr, z1, r)
        dzo_ref[...] = dz1
        y = y_ref[...].astype(F32)
        r1 = _rstd(y)
        yr = y * r1
        wpo = wpo_ref[...]
        g = _mod_row(mc_ref, mx_ref, 2, is_ctx)
        _acc_mod(sm_ref, 4, is_ctx, _colsum(dz1 * (yr * wpo)))
        dn1 = dz1 * g
        _acc_row(sm_ref, 7, _colsum(dn1 * yr))
        dy_ref[...] = _rms_bwd(dn1 * wpo, y, r1).astype(BF16)

    return pl.pallas_call(
        body, name="mid_bwd", grid=(t // TR,),
        in_specs=[_row(D)] * 5 + [_full((1, D)), _full((1, D)), _full((8, D)), _full((8, D))],
        out_specs=[_row(D), _row(D), _full((8, D))],
        out_shape=[jax.ShapeDtypeStruct((t, D), F32), jax.ShapeDtypeStruct((t, D), BF16),
                   jax.ShapeDtypeStruct((8, D), F32)],
        compiler_params=_cp("arbitrary"),
    )(dh2, dz, z, z1, y1, nw_post, nw_pre, modc, modx)


def _pre_bwd(dh1, dz, z, nw, modc, modx):
    t = z.shape[0]

    def body(dh_ref, dz_ref, z_ref, w_ref, mc_ref, mx_ref, dzo_ref, sm_ref):
        i = pl.program_id(0)
        is_ctx = i < NCT

        @pl.when(i == 0)
        def _():
            sm_ref[...] = jnp.zeros_like(sm_ref)

        dh = dh_ref[...].astype(F32)
        x = z_ref[...]
        r = _rstd(x)
        xr = x * r
        w = w_ref[...]
        _acc_mod(sm_ref, 0, is_ctx, _colsum(dh))
        _acc_mod(sm_ref, 2, is_ctx, _colsum(dh * (xr * w)))
        dn = dh * (1.0 + _mod_row(mc_ref, mx_ref, 1, is_ctx))
        _acc_row(sm_ref, 4, _colsum(dn * xr))
        dzo_ref[...] = dz_ref[...] + _rms_bwd(dn * w, x, r)

    return pl.pallas_call(
        body, name="pre_bwd", grid=(t // TR,),
        in_specs=[_row(D)] * 3 + [_full((1, D)), _full((8, D)), _full((8, D))],
        out_specs=[pl.BlockSpec((TR, D), lambda i: (jnp.maximum(i - NCT, 0), 0)), _full((8, D))],
        out_shape=[jax.ShapeDtypeStruct((t - CTX, D), F32), jax.ShapeDtypeStruct((8, D), F32)],
        compiler_params=_cp("arbitrary"),
    )(dh1, dz, z, nw, modc, modx)


def _merge_bwd(dm, p, u1, u2):
    t = dm.shape[0]

    def body(dm_ref, a0, a1, a2, b0, b1, b2, u1_ref, u2_ref, du1_ref, du2_ref, dg_ref):
        dm_ = dm_ref[...].astype(F32)
        s1 = _sig(_gate_window((a0, a1, a2)))
        s2 = _sig(_gate_window((b0, b1, b2)))
        du1_ref[...] = (dm_ * s1).astype(BF16)
        du2_ref[...] = (dm_ * s2).astype(BF16)
        dg_ref[:, :GW] = (dm_ * u1_ref[...].astype(F32) * s1 * (1.0 - s1)).astype(BF16)
        dg_ref[:, GW:] = (dm_ * u2_ref[...].astype(F32) * s2 * (1.0 - s2)).astype(BF16)

    return pl.pallas_call(
        body, name="merge_bwd", grid=(t // TR,),
        in_specs=[_row(GW)] + _gate_window_specs(GATE_HG0) + _gate_window_specs(GATE_GLA0) + [_row(GW), _row(GW)],
        out_specs=[_row(GW), _row(GW), _row(2 * GW)],
        out_shape=[jax.ShapeDtypeStruct((t, GW), BF16), jax.ShapeDtypeStruct((t, GW), BF16),
                   jax.ShapeDtypeStruct((t, 2 * GW), BF16)],
        compiler_params=_cp("parallel"),
    )(dm, p, p, p, p, p, p, u1, u2)


def _post_bwd(dy_hg, dy_gla, o_fw, o_bw, p, onw):
    t = o_fw.shape[0]

    def body(d1_ref, d2_ref, of_ref, ob_ref, g1_ref, g2_ref, w_ref, do_ref, dg_ref, sm_ref):
        @pl.when(pl.program_id(0) == 0)
        def _():
            sm_ref[...] = jnp.zeros_like(sm_ref)

        for h in range(NH):
            sl = slice(h * DH, (h + 1) * DH)
            gs = slice((h % (NH // 2)) * DH, (h % (NH // 2) + 1) * DH)
            g_ref, d_ref = (g1_ref, d1_ref) if h < NH // 2 else (g2_ref, d2_ref)
            o = of_ref[:, sl] + ob_ref[:, sl]
            r = _rstd(o)
            orr = o * r
            w = w_ref[:, sl]
            gt = g_ref[:, gs].astype(F32)
            dy = d_ref[:, gs].astype(F32)
            dg_ref[:, sl] = (dy * (orr * w) * _dsilu(gt)).astype(BF16)
            dn = dy * _silu(gt)
            sm_ref[0:1, sl] += _colsum(dn * orr)
            do_ref[:, sl] = _rms_bwd(dn * w, o, r)

    return pl.pallas_call(
        body, name="post_bwd", grid=(t // TR,),
        in_specs=[_row(HW), _row(HW), _row(D), _row(D), _rowcol(HW, MAIN0 // HW + 4), _rowcol(HW, MAIN0 // HW + 8),
                  _full((1, D))],
        out_specs=[_row(D), _row(D), _full((8, D))],
        out_shape=[jax.ShapeDtypeStruct((t, D), F32), jax.ShapeDtypeStruct((t, D), BF16),
                   jax.ShapeDtypeStruct((8, D), F32)],
        compiler_params=_cp("arbitrary"),
    )(dy_hg, dy_gla, o_fw, o_bw, p, p, onw)


def _gates_bwd(p, hg_lb, wgk, bgk, dgm, dgo, dq_f, dq_b, dv_f, dv_b, dk_f, dk_b, dg_f, dg_b):
    t = p.shape[0]
    seg = lambda j: _rowcol(HW, MAIN0 // HW + j)

    def body(hq_ref, hf_ref, hb_ref, lr_ref, lb_ref, wgk_ref, bgk_ref, dgm_ref, dgo_ref,
             dqf_ref, dqb_ref, dvf_ref, dvb_ref, dkf_ref, dkb_ref, dgf_ref, dgb_ref,
             dp_ref, dlb_ref, dw_ref, db_ref):
        @pl.when(pl.program_id(0) == 0)
        def _():
            dlb_ref[...] = jnp.zeros_like(dlb_ref)
            dw_ref[...] = jnp.zeros_like(dw_ref)
            db_ref[...] = jnp.zeros_like(db_ref)

        c0 = MAIN0

        def put(j, val):
            dp_ref[:, c0 + j * HW:c0 + (j + 1) * HW] = val.astype(BF16)

        dq = dqf_ref[...].astype(F32) + dqb_ref[...].astype(F32)
        dv = dvf_ref[...].astype(F32) + dvb_ref[...].astype(F32)
        put(0, dq[:, :HW] * _dsilu(hq_ref[...].astype(F32)))
        put(1, dv[:, :HW])
        put(5, dq[:, HW:] * (DH ** -0.5))
        put(7, dv[:, HW:])
        put(6, dkf_ref[:, HW:].astype(F32) + dkb_ref[:, HW:].astype(F32))
        dp_ref[:, c0 + 4 * HW:c0 + 5 * HW] = dgo_ref[:, :HW]
        dp_ref[:, c0 + 8 * HW:c0 + 9 * HW] = dgo_ref[:, HW:]
        lr = lr_ref[...].astype(BF16)
        xg = _dot(lr, wgk_ref[...], NN) + bgk_ref[...]
        dxg = []
        for d, (raw_ref, dk_ref, dg_ref) in enumerate(((hf_ref, dkf_ref, dgf_ref), (hb_ref, dkb_ref, dgb_ref))):
            lbd = _hg_lb(lb_ref, d)
            s = _sig(raw_ref[...].astype(F32))
            f = lbd + (1.0 - lbd) * s
            df = dg_ref[:, :HW] / f - dk_ref[:, :HW].astype(F32)
            put(2 + d, df * (1.0 - lbd) * s * (1.0 - s))
            dlb_ref[d:d + 1, :] += _colsum(df * (1.0 - s)) * (lbd * (1.0 - lbd))
            dxg.append(dg_ref[:, HW:] * (1.0 / GLA_NORM) * _sig(-xg[:, d * HW:(d + 1) * HW]))
        dxg = jnp.concatenate(dxg, axis=1)
        db_ref[0:1, :] += _colsum(dxg)
        dxg_b = dxg.astype(BF16)
        dw_ref[...] += _dot(lr, dxg_b, TN)
        dlr = _dot(dxg_b, wgk_ref[...], NT)
        dp_ref[:, LR0:LR0 + DH] = (dlr + dgm_ref[:, :DH].astype(F32)).astype(BF16)
        dp_ref[:, LR0 + DH:GATE_GLA0] = dgm_ref[:, DH:D]
        dp_ref[:, GATE_GLA0:GATE_GLA0 + DH] = dgm_ref[:, D:GW] + dgm_ref[:, GW:GW + DH]
        dp_ref[:, GATE_GLA0 + DH:GATE_GLA0 + GW] = dgm_ref[:, GW + DH:]
        dp_ref[:, GATE_GLA0 + GW:] = jnp.zeros((TR, W_IN_COLS - GATE_GLA0 - GW), BF16)

    return pl.pallas_call(
        body, name="gates_bwd", grid=(t // TR,),
        in_specs=[seg(0), seg(2), seg(3), _rowcol(DH, LR0 // DH), _full((2, 2, HW)), _full((DH, D)), _full((1, D)),
                  _row(2 * GW), _row(D)] + [_row(D)] * 8,
        out_specs=[_row(W_IN_COLS), _full((8, HW)), _full((DH, D)), _full((8, D))],
        out_shape=[jax.ShapeDtypeStruct((t, W_IN_COLS), BF16), jax.ShapeDtypeStruct((8, HW), F32),
                   jax.ShapeDtypeStruct((DH, D), F32), jax.ShapeDtypeStruct((8, D), F32)],
        compiler_params=_cp("arbitrary"),
    )(p, p, p, p, hg_lb, wgk, bgk, dgm, dgo, dq_f, dq_b, dv_f, dv_b, dk_f, dk_b, dg_f, dg_b)


def _scan_consts(rev):
    r = lax.broadcasted_iota(jnp.int32, (CHUNK, CHUNK), 0)
    u = lax.broadcasted_iota(jnp.int32, (CHUNK, CHUNK), 1)
    rp = lax.broadcasted_iota(jnp.int32, (CHUNK, 1), 0)
    if rev:
        r, u, rp = CHUNK - 1 - r, CHUNK - 1 - u, CHUNK - 1 - rp
    tri = jnp.where(u <= r, 1.0, 0.0).astype(F32)
    tri_t = jnp.where(r <= u, 1.0, 0.0).astype(F32)
    lv = []
    for b in LEVELS:
        sh = b.bit_length() - 1
        pair = ((r >> sh) == (u >> sh) + 1) & (((u >> sh) & 1) == 0)
        pair_t = ((u >> sh) == (r >> sh) + 1) & (((r >> sh) & 1) == 0)
        tside = ((rp >> sh) & 1) == 1
        lv.append((pair, pair_t, tside))
    bd = LEVELS[-1].bit_length() - 1
    diag = ((r >> bd) == (u >> bd)) & (u <= r)
    diag_t = ((r >> bd) == (u >> bd)) & (r <= u)
    return tri, tri_t, lv, diag, diag_t


def _row_of(pos, rev):
    return CHUNK - 1 - pos if rev else pos


def _chunk_terms(cum, b_scr, consts, rev):
    _, _, lv, _, _ = consts
    terms = []
    for b, (_, _, tside) in zip(LEVELS, lv):
        pieces = []
        for j in range(CHUNK // (2 * b)):
            row = _row_of(2 * b * j + b - 1, rev)
            pieces.append(jnp.broadcast_to(b_scr[row:row + 1, :], (2 * b, DH)))
        if rev:
            pieces = pieces[::-1]
        bnd = pieces[0] if len(pieces) == 1 else jnp.concatenate(pieces, axis=0)
        w = jnp.exp(jnp.minimum(jnp.where(tside, cum - bnd, bnd - cum), 0.0))
        wq = jnp.where(tside, w, 0.0)
        wk = jnp.where(tside, 0.0, w)
        terms.append((wq, wk))
    b = LEVELS[-1]
    pieces = []
    for j in range(CHUNK // b):
        if j == 0:
            pieces.append(jnp.zeros((b, DH), F32))
        else:
            row = _row_of(b * j - 1, rev)
            pieces.append(jnp.broadcast_to(b_scr[row:row + 1, :], (b, DH)))
    if rev:
        pieces = pieces[::-1]
    start = jnp.concatenate(pieces, axis=0)
    wq = jnp.exp(jnp.minimum(cum - start, 0.0))
    wk = jnp.exp(jnp.minimum(start - cum, EXP_CLAMP))
    terms.append((wq, wk))
    return terms


def _run_staged(units):
    live = list(units)
    while live:
        nxt = []
        for u in live:
            try:
                next(u)
                nxt.append(u)
            except StopIteration:
                pass
        live = nxt


SCAN_TB = 256
SCAN_CB = SCAN_TB // CHUNK


def _block_order(i, ntb, rev):
    nctx = CTX // SCAN_TB
    if not rev:
        return i
    return jnp.where(i < nctx, nctx - 1 - i, ntb - 1 - (i - nctx))


def _chunk_in_block(j, rev):
    return SCAN_CB - 1 - j if rev else j


def _scan_fwd(q, k, v, g, rev):
    t = q.shape[0]
    nc = t // CHUNK
    hpb = SCAN_HEADS_FWD

    def body(q_ref, k_ref, v_ref, g_ref, o_ref, st_ref, s_scr, b_scr):
        consts = _scan_consts(rev)
        _, _, lv, diag, _ = consts
        masks = [pair for pair, _, _ in lv] + [diag]

        @pl.when(pl.program_id(1) == 0)
        def _():
            s_scr[...] = jnp.zeros_like(s_scr)

        tri = consts[0]
        state = {hh: s_scr[hh] for hh in range(hpb)}

        def unit(hh, j):
            sl = slice(hh * DH, (hh + 1) * DH)
            c = _chunk_in_block(j, rev)
            rows = slice(c * CHUNK, (c + 1) * CHUNK)
            b_ref = b_scr.at[hh * SCAN_CB + j]
            qc, kc, vc, gc = q_ref[rows, sl], k_ref[rows, sl], v_ref[rows, sl], g_ref[rows, sl]
            cum = _split_dot(tri, gc)
            b_ref[...] = cum
            yield
            terms = _chunk_terms(cum, b_ref, consts, rev)
            ops = [((qc * wq).astype(BF16), (kc * wk).astype(BF16)) for wq, wk in terms]
            tot = _colsum(gc)
            qe = (qc * jnp.exp(cum)).astype(BF16)
            ke = (kc * jnp.exp(tot - cum)).astype(BF16)
            vb = vc.astype(BF16)
            yield
            scs = [_dot(qt, kt, NT) for qt, kt in ops]
            kv = _dot(vb, ke, TN)
            yield
            a = jnp.zeros((CHUNK, CHUNK), F32)
            for sc, m in zip(scs, masks):
                a = a + jnp.where(m, sc, 0.0)
            o_intra = _dot(a.astype(BF16), vb, NN)
            yield
            st = state[hh]
            st_ref[hh, c] = st
            o_ref[rows, sl] = o_intra + _dot(qe, st.astype(BF16), NT)
            state[hh] = st * jnp.exp(tot) + kv
            yield

        _run_staged([unit(hh, j) for hh in range(hpb) for j in range(SCAN_CB)])
        for hh in range(hpb):
            s_scr[hh] = state[hh]

    ntb = t // SCAN_TB
    col = pl.BlockSpec((SCAN_TB, hpb * DH), lambda h, i: (_block_order(i, ntb, rev), h))
    return pl.pallas_call(
        body, name="scan_fwd_" + ("bw" if rev else "fw"), grid=(NH // hpb, ntb),
        in_specs=[col] * 4,
        out_specs=[col, pl.BlockSpec((hpb, SCAN_CB, DH, DH), lambda h, i: (h, _block_order(i, ntb, rev), 0, 0))],
        out_shape=[jax.ShapeDtypeStruct((t, D), F32), jax.ShapeDtypeStruct((NH, nc, DH, DH), F32)],
        scratch_shapes=[pltpu.VMEM((hpb, DH, DH), F32), pltpu.VMEM((hpb * SCAN_CB, CHUNK, DH), F32)],
        compiler_params=_cp("parallel", "arbitrary"),
    )(q, k, v, g)


def _scan_bwd(q, k, v, g, do, states, rev):
    t = q.shape[0]
    nc = t // CHUNK
    hpb = SCAN_HEADS_BWD

    def body(q_ref, k_ref, v_ref, g_ref, do_ref, st_ref, dq_ref, dk_ref, dv_ref, dg_ref, ds_scr, b_scr):
        consts = _scan_consts(rev)
        _, tri_t, lv, diag, diag_t = consts
        masks = [(pair, pair_t) for pair, pair_t, _ in lv] + [(diag, diag_t)]
        @pl.when(pl.program_id(1) == 0)
        def _():
            ds_scr[...] = jnp.zeros_like(ds_scr)

        tri = consts[0]
        dstate = {hh: ds_scr[hh] for hh in range(hpb)}

        def unit(hh, jj):
            sl = slice(hh * DH, (hh + 1) * DH)
            c = _chunk_in_block(SCAN_CB - 1 - jj, rev)
            rows = slice(c * CHUNK, (c + 1) * CHUNK)
            b_ref = b_scr.at[hh * SCAN_CB + jj]
            qc, kc, vc, gc = q_ref[rows, sl], k_ref[rows, sl], v_ref[rows, sl], g_ref[rows, sl]
            dob = do_ref[rows, sl].astype(BF16)
            vb = vc.astype(BF16)
            cum = _split_dot(tri, gc)
            b_ref[...] = cum
            da = _dot(dob, vb, NT)
            da_t = _dot(vb, dob, NT)
            yield
            terms = _chunk_terms(cum, b_ref, consts, rev)
            ops = [((qc * wq).astype(BF16), (kc * wk).astype(BF16)) for wq, wk in terms]
            tot = _colsum(gc)
            e_tot = jnp.exp(tot)
            e_b = jnp.exp(cum)
            e_t = jnp.exp(tot - cum)
            qeb = (qc * e_b).astype(BF16)
            keb = (kc * e_t).astype(BF16)
            dal = [(jnp.where(m, da, 0.0).astype(BF16), jnp.where(m_t, da_t, 0.0).astype(BF16)) for m, m_t in masks]
            yield
            ats = [_dot(ktb, qtb, NT) for qtb, ktb in ops]
            dqts = [_dot(d, ktb, NN) for (d, _), (_, ktb) in zip(dal, ops)]
            dkts = [_dot(d_t, qtb, NN) for (_, d_t), (qtb, _) in zip(dal, ops)]
            qd = _dot(dob, qeb, TN)
            yield
            a_t = jnp.zeros((CHUNK, CHUNK), F32)
            dq = jnp.zeros((CHUNK, DH), F32)
            dk = jnp.zeros((CHUNK, DH), F32)
            db = jnp.zeros((CHUNK, DH), F32)
            for at, dqt, dkt, (wq, wk), (qtb, ktb), (_, m_t) in zip(ats, dqts, dkts, terms, ops, masks):
                a_t = a_t + jnp.where(m_t, at, 0.0)
                dq = dq + dqt * wq
                dk = dk + dkt * wk
                db = db + dqt * qtb.astype(F32) - dkt * ktb.astype(F32)
            dv_intra = _dot(a_t.astype(BF16), dob, NN)
            st = st_ref[hh, c]
            stb = st.astype(BF16)
            dqe = _dot(dob, stb, NN)
            yield
            dst = dstate[hh]
            dstb = dst.astype(BF16)
            dstate[hh] = dst * e_tot + qd
            dv_ref[rows, sl] = (dv_intra + _dot(keb, dstb, NT)).astype(BF16)
            dke = _dot(vb, dstb, NN)
            yield
            qe = qeb.astype(F32)
            ke = keb.astype(F32)
            dq_ref[rows, sl] = (dq + dqe * e_b).astype(BF16)
            dk_ref[rows, sl] = (dk + dke * e_t).astype(BF16)
            db = db + dqe * qe - dke * ke
            dtot = _colsum(dstb.astype(F32) * stb.astype(F32)) * e_tot + _colsum(dke * ke)
            dg_ref[rows, sl] = _split_dot(tri_t, db) + dtot
            yield

        _run_staged([unit(hh, jj) for hh in range(hpb) for jj in range(SCAN_CB)])
        for hh in range(hpb):
            ds_scr[hh] = dstate[hh]

    ntb = t // SCAN_TB
    blk = lambda i: _block_order(ntb - 1 - i, ntb, rev)
    col = pl.BlockSpec((SCAN_TB, hpb * DH), lambda h, i: (blk(i), h))
    out = jax.ShapeDtypeStruct((t, D), F32)
    outb = jax.ShapeDtypeStruct((t, D), BF16)
    return pl.pallas_call(
        body, name="scan_bwd_" + ("bw" if rev else "fw"), grid=(NH // hpb, ntb),
        in_specs=[col] * 5 + [pl.BlockSpec((hpb, SCAN_CB, DH, DH), lambda h, i: (h, blk(i), 0, 0))],
        out_specs=[col] * 4,
        out_shape=[outb] * 3 + [out],
        scratch_shapes=[pltpu.VMEM((hpb, DH, DH), F32), pltpu.VMEM((hpb * SCAN_CB, CHUNK, DH), F32)],
        compiler_params=_cp("parallel", "arbitrary"),
    )(q, k, v, g, do, states)


W_IN_GRAD_CHUNKS = (("a", (0, 512)), ("b", (0, 128)), ("b", (128, 512)))
W_IN_REF = 6688


def _layout_w_in(w):
    return jnp.pad(w, ((0, 0), (0, W_IN_COLS - W_IN_REF)))


def _unlayout_w_in(d):
    return d[:, :W_IN_REF]


def _gate_cols(w):
    return jnp.pad(w, ((0, 0), (GOFF, GW - GOFF - D)))


def _gate_rows(w):
    return jnp.pad(w, ((GOFF, GW - GOFF - D), (0, 0)))


def _layout_wgk(w):
    r = w.shape[1]
    top = jnp.concatenate([w[0], jnp.zeros_like(w[0])], axis=1)
    bot = jnp.concatenate([jnp.zeros_like(w[1]), w[1]], axis=1)
    return jnp.concatenate([top, bot, jnp.zeros((DH - 2 * r, D), w.dtype)], axis=0)


def _unlayout_wgk(d, r=16):
    return jnp.stack([d[:r, :HW], d[r:2 * r, HW:]])


def _local_step(z, target, modc, modx, norms, onw, hg_lb, wgk, bgk, w_in, get_mix, get_ffn, send):
    n_pre1, n_post1, n_pre2, n_post2 = norms
    t = z.shape[0]
    tm = 768 if t % 768 == 0 else 256
    h1 = _prenorm(z, n_pre1, modc, modx, 0, 1, "prenorm1")
    p = _matmul(h1, w_in, NN, BF16, "mm_in", tm, 512, D)
    q, v, k_f, k_b, g_f, g_b = _gates_fwd(p, hg_lb, wgk, bgk)
    o_f, st_f = _scan_fwd(q, k_f, v, g_f, False)
    o_b, st_b = _scan_fwd(q, k_b, v, g_b, True)
    y = _post_fwd(o_f, o_b, p, onw)
    w_br_hg, w_br_gla, w_out = get_mix(y)
    u1 = _matmul(y, w_br_hg, NN, BF16, "mm_br_hg", tm, GW, HW, a_off=0)
    u2 = _matmul(y, w_br_gla, NN, BF16, "mm_br_gla", tm, GW, HW, a_off=1)
    merged = _merge_fwd(p, u1, u2)
    y1 = _matmul(merged, w_out, NN, BF16, "mm_out", tm, 512, GW)
    z1, h2 = _mid_fwd(z, y1, n_post1, n_pre2, modc, modx)
    w_gu_t, w_down = get_ffn(h2)
    uv = _matmul(h2, w_gu_t, NT, BF16, "mm_gu", tm, D_FF // 2, D)
    act = _swiglu_fwd(uv)
    y2 = _matmul(act, w_down, NN, BF16, "mm_down", tm, 512, D_FF // 2)
    dz, dy2, loss_vec, sm_final = _final(z1, y2, target, n_post2, modc, modx)
    dact = _matmul(dy2, w_down, NT, BF16, "mm_down_dx", tm, D_FF // 2, D)
    d_w_down = _matmul(act, dy2, TN, BF16, "mm_down_dw", D_FF // 2, 512, t)
    duv = _swiglu_bwd(uv, dact)
    dh2 = _matmul(duv, w_gu_t, NN, BF16, "mm_gu_dx", tm, 512, D_FF // 2)
    d_w_gate_t = _matmul(duv, h2, TN, BF16, "mm_gate_dw", D_FF // 2, 512, t, a_off=0, m_out=D_FF)
    d_w_up_t = _matmul(duv, h2, TN, BF16, "mm_up_dw", D_FF // 2, 512, t, a_off=2, m_out=D_FF)
    dh2 = send(("w_down", "w_gate_t", "w_up_t"), (d_w_down, d_w_gate_t, d_w_up_t), dh2)
    dz, dy1, sm_mid = _mid_bwd(dh2, dz, z, z1, y1, n_post1, n_pre2, modc, modx)
    dmerged = _matmul(dy1, w_out, NT, BF16, "mm_out_dx", tm, GW, D)
    d_w_out = _matmul(merged, dy1, TN, BF16, "mm_out_dw", GW, 512, t)
    du1, du2, dgm = _merge_bwd(dmerged, p, u1, u2)
    dy_hg = _matmul(du1, w_br_hg, NT, BF16, "mm_br_hg_dx", tm, HW, GW)
    dy_gla = _matmul(du2, w_br_gla, NT, BF16, "mm_br_gla_dx", tm, HW, GW)
    d_w_br_hg = _matmul(y, du1, TN, BF16, "mm_br_hg_dw", HW, GW, t, a_off=0, m_out=HW)
    d_w_br_gla = _matmul(y, du2, TN, BF16, "mm_br_gla_dw", HW, GW, t, a_off=1, m_out=HW)
    dy_hg = send(("w_out", "w_br_hg", "w_br_gla"), (d_w_out, d_w_br_hg, d_w_br_gla), dy_hg)
    do, dgo, sm_post = _post_bwd(dy_hg, dy_gla, o_f, o_b, p, onw)
    dq_f, dk_f, dv_f, dg_f = _scan_bwd(q, k_f, v, g_f, do, st_f, False)
    dq_b, dk_b, dv_b, dg_b = _scan_bwd(q, k_b, v, g_b, do, st_b, True)
    dp, d_lb, d_wgk, d_bgk = _gates_bwd(p, hg_lb, wgk, bgk, dgm, dgo, dq_f, dq_b, dv_f, dv_b, dk_f, dk_b, dg_f, dg_b)
    d_w_in_a = _matmul(h1, dp, TN, BF16, "mm_in_dw_a", 512, 512, t, a_off=0, m_out=D // 2)
    dp = send(("w_in_a",), (d_w_in_a,), dp)
    d_w_in_b = _matmul(h1, dp, TN, BF16, "mm_in_dw_b", 512, 512, t, a_off=1, m_out=D // 2)
    dp = send(("w_in_b",), (d_w_in_b,), dp)
    dh1 = _matmul(dp, w_in, NT, BF16, "mm_in_dx", tm, 512, 1024)
    grad_x, sm_pre = _pre_bwd(dh1, dz, z, n_pre1, modc, modx)
    return dict(loss_vec=loss_vec, grad_x=grad_x, sm_final=sm_final, sm_mid=sm_mid, sm_post=sm_post, sm_pre=sm_pre,
                d_lb=d_lb, d_wgk=d_wgk, d_bgk=d_bgk)


MESH = pl.DeviceIdType.MESH
ANY = pl.BlockSpec(memory_space=pl.ANY)
N_REL = N_DEV - 1


def _place():
    return lax.axis_index("x"), lax.axis_index("y"), lax.axis_index("c")


def _slot(p):
    return 4 * p[0] + 2 * p[1] + p[2]


def _all_gather(arrays, name):
    n = len(arrays)

    def body(*refs):
        ins, outs = refs[:n], refs[n:2 * n]
        send_sems, recv_sems, local_sems = refs[2 * n:]
        x, y, c = _place()
        me, sibling = (x, y, c), (x, y, 1 - c)
        chips = [(1 - x, y), (x, 1 - y), (1 - x, 1 - y)]

        def copy(a, k, block, to, src=None):
            dst = outs[a].at[_slot(block)]
            return pltpu.make_async_remote_copy(
                src_ref=dst if src is None else src, dst_ref=dst,
                send_sem=send_sems.at[N_REL * a + k], recv_sem=recv_sems.at[N_REL * a + k],
                device_id=to, device_id_type=MESH)

        mine = [pltpu.make_async_copy(ins[a], outs[a].at[_slot(me)], local_sems.at[a]) for a in range(n)]
        for cp in mine:
            cp.start()
        first = []
        for a in range(n):
            first.append(copy(a, 0, me, sibling, src=ins[a]))
            first += [copy(a, 1 + j, me, (*chip, c), src=ins[a]) for j, chip in enumerate(chips)]
        for cp in first:
            cp.start()
        passed = []
        for j, chip in enumerate(chips):
            for a in range(n):
                copy(a, 1 + j, (*chip, c), me).wait_recv()
                fwd = copy(a, 4 + j, (*chip, c), sibling)
                fwd.start()
                passed.append(fwd)
        for a in range(n):
            copy(a, 0, sibling, me).wait_recv()
        for j, chip in enumerate(chips):
            for a in range(n):
                copy(a, 4 + j, (*chip, 1 - c), me).wait_recv()
        for cp in first + passed:
            cp.wait_send()
        for cp in mine:
            cp.wait()

    return pl.pallas_call(
        body, name=name,
        in_specs=[ANY] * n, out_specs=[ANY] * n,
        out_shape=[jax.ShapeDtypeStruct((N_DEV,) + a.shape, a.dtype) for a in arrays],
        scratch_shapes=[pltpu.SemaphoreType.DMA((N_REL * n,)), pltpu.SemaphoreType.DMA((N_REL * n,)),
                        pltpu.SemaphoreType.DMA((n,))],
    )(*arrays)


def _exchange(arrays, name):
    n = len(arrays)

    def body(*refs):
        ins, outs = refs[:n], refs[n:2 * n]
        send_sems, recv_sems, local_sems = refs[2 * n:]
        x, y, c = _place()
        me = _slot((x, y, c))
        mine = [pltpu.make_async_copy(ins[a].at[me], outs[a].at[me], local_sems.at[a]) for a in range(n)]
        for cp in mine:
            cp.start()
        copies = []
        for a in range(n):
            for k in range(1, N_DEV):
                flip = lambda v, bit: 1 - v if bit else v
                peer = (flip(x, k & 4), flip(y, k & 2), flip(c, k & 1))
                copies.append(pltpu.make_async_remote_copy(
                    src_ref=ins[a].at[_slot(peer)], dst_ref=outs[a].at[me],
                    send_sem=send_sems.at[N_REL * a + k - 1], recv_sem=recv_sems.at[N_REL * a + k - 1],
                    device_id=peer, device_id_type=MESH))
                copies[-1].start()
        i = 0
        for a in range(n):
            for k in range(1, N_DEV):
                flip = lambda v, bit: 1 - v if bit else v
                peer = (flip(x, k & 4), flip(y, k & 2), flip(c, k & 1))
                pltpu.make_async_remote_copy(
                    src_ref=ins[a].at[_slot(peer)], dst_ref=outs[a].at[_slot(peer)],
                    send_sem=send_sems.at[N_REL * a + k - 1], recv_sem=recv_sems.at[N_REL * a + k - 1],
                    device_id=peer, device_id_type=MESH).wait_recv()
                i += 1
        for cp in copies:
            cp.wait_send()
        for cp in mine:
            cp.wait()

    return pl.pallas_call(
        body, name=name,
        in_specs=[ANY] * n, out_specs=[ANY] * n,
        out_shape=[jax.ShapeDtypeStruct(a.shape, a.dtype) for a in arrays],
        scratch_shapes=[pltpu.SemaphoreType.DMA((N_REL * n,)), pltpu.SemaphoreType.DMA((N_REL * n,)),
                        pltpu.SemaphoreType.DMA((n,))],
    )(*arrays)


HBM = pl.BlockSpec(memory_space=pltpu.HBM)
SEM = pl.BlockSpec(memory_space=pltpu.SEMAPHORE)
EFFECT = pltpu.SideEffectType.DATAFLOW_SIDE_EFFECTING


def _peer_of(x, y, c, k):
    flip = lambda v, bit: 1 - v if bit else v
    return flip(x, k & 4), flip(y, k & 2), flip(c, k & 1)


def _view_whole(src, slot):
    return src


def _view_block(src, slot):
    return src.at[slot]


W_IN_SHARD = W_IN_REF // N_DEV


def _view_window(rows):
    def view(src, slot):
        col0 = pl.multiple_of((W_IN_SHARD * slot // DH) * DH, DH)
        return src.at[pl.ds(rows[0], rows[1] - rows[0]), pl.ds(col0, D)]
    return view


def _split_copies(view, srcs, lands, send_sems, recv_sems, local_sems):
    x, y, c = _place()
    me = _slot((x, y, c))
    local, sends, waits = [], [], []
    for a, (src, land) in enumerate(zip(srcs, lands)):
        local.append(pltpu.make_async_copy(view(src, me), land.at[me], local_sems.at[a]))
        for k in range(1, N_DEV):
            peer = _peer_of(x, y, c, k)
            mine = view(src, _slot(peer))
            sems = dict(send_sem=send_sems.at[N_REL * a + k - 1], recv_sem=recv_sems.at[N_REL * a + k - 1],
                        device_id=peer, device_id_type=MESH)
            sends.append(pltpu.make_async_remote_copy(src_ref=mine, dst_ref=land.at[me], **sems))
            waits.append(pltpu.make_async_remote_copy(src_ref=mine, dst_ref=land.at[_slot(peer)], **sems))
    return local, sends, waits


def _split_start(view, land_shapes, srcs, name, after):
    n = len(srcs)
    lands = [lax.empty(shp, s.dtype) for shp, s in zip(land_shapes, srcs)]

    def body(*refs):
        src_refs, land_refs = refs[:n], refs[n:2 * n]
        send_sems, recv_sems, local_sems = refs[2 * n + 1:2 * n + 4]
        token = refs[-1]
        local, sends, _ = _split_copies(view, src_refs, land_refs, send_sems, recv_sems, local_sems)
        for cp in local + sends:
            cp.start()
        token[...] = jnp.zeros_like(token)

    hbm = lambda a: pltpu.with_memory_space_constraint(a, pltpu.HBM)
    out = pl.pallas_call(
        body, name=name,
        out_shape=(pltpu.SemaphoreType.DMA((N_REL * n,)), pltpu.SemaphoreType.DMA((N_REL * n,)),
                   pltpu.SemaphoreType.DMA((n,)),
                   *[pltpu.HBM(s.shape, s.dtype) for s in srcs], *[pltpu.HBM(l.shape, l.dtype) for l in lands],
                   jax.ShapeDtypeStruct((8, DH), F32)),
        in_specs=[HBM] * (2 * n) + [ANY],
        out_specs=(SEM, SEM, SEM, *([HBM] * (2 * n)), pl.BlockSpec(memory_space=pltpu.VMEM)),
        input_output_aliases={i: 3 + i for i in range(2 * n)},
        compiler_params=pltpu.CompilerParams(has_side_effects=EFFECT),
    )(*[hbm(s) for s in srcs], *[hbm(l) for l in lands], after)
    handle = dict(view=view, n=n, sems=out[:3], srcs=list(out[3:3 + n]), lands=list(out[3 + n:3 + 2 * n]))
    return handle, out[-1]


def _split_wait(handle, name, after, srcs=None):
    view, n, sems, lands = handle["view"], handle["n"], handle["sems"], handle["lands"]
    srcs = handle["srcs"] if srcs is None else srcs
    afters = list(after) if isinstance(after, (list, tuple)) else [after]

    def body(*refs):
        src_refs, land_refs = refs[:n], refs[n:2 * n]
        send_sems, recv_sems, local_sems = refs[2 * n:2 * n + 3]
        local, _, waits = _split_copies(view, src_refs, land_refs, send_sems, recv_sems, local_sems)
        for cp in waits:
            cp.wait_send()
            cp.wait_recv()
        for cp in local:
            cp.wait()

    out = pl.pallas_call(
        body, name=name,
        out_shape=(*[pltpu.HBM(s.shape, s.dtype) for s in srcs], *[pltpu.HBM(l.shape, l.dtype) for l in lands]),
        in_specs=[HBM] * (2 * n) + [SEM, SEM, SEM] + [ANY] * len(afters),
        out_specs=tuple([HBM] * (2 * n)),
        input_output_aliases={i: i for i in range(2 * n)},
        compiler_params=pltpu.CompilerParams(has_side_effects=EFFECT),
    )(*srcs, *lands, *sems, *afters)
    handle["srcs"] = list(out[:n])
    return list(out[n:])


def _tie(x, token, name):
    def body(x_ref, t_ref, o_ref):
        pass

    return pl.pallas_call(
        body, name=name, out_shape=jax.ShapeDtypeStruct(x.shape, x.dtype),
        in_specs=[ANY, ANY], out_specs=ANY, input_output_aliases={0: 0},
    )(x, token)


def _mod_fwd(a, w, b):
    def body(a_ref, w_ref, b_ref, o_ref):
        o_ref[...] = _dot(_silu(a_ref[...]), w_ref[...], NN, precision=HI) + b_ref[...]

    return pl.pallas_call(
        body, name="mod_fwd", out_shape=jax.ShapeDtypeStruct((a.shape[0], w.shape[1]), F32),
        compiler_params=pltpu.CompilerParams(vmem_limit_bytes=VMEM_LIMIT),
    )(a, w, b)


def _mod_bwd(a, d, w):
    def body(a_ref, d_ref, w_ref, dw_ref, dc_ref):
        av = a_ref[...]
        dv = d_ref[...]
        dw_ref[...] = _dot(_silu(av), dv, TN, precision=HI)
        da = _dot(dv[0:8, :], w_ref[...], NT, precision=HI) * _dsilu(av[0:8, :])
        row = lax.broadcasted_iota(jnp.int32, da.shape, 0)
        dc_ref[...] = jnp.where(row == 0, da, 0.0)

    return pl.pallas_call(
        body, name="mod_bwd",
        out_shape=[jax.ShapeDtypeStruct(w.shape, F32), jax.ShapeDtypeStruct((8, w.shape[0]), F32)],
        compiler_params=pltpu.CompilerParams(vmem_limit_bytes=VMEM_LIMIT),
    )(a, d, w)


def _sum_devices(g):
    def body(g_ref, o_ref):
        acc = g_ref[0]
        for i in range(1, g.shape[0]):
            acc = acc + g_ref[i]
        o_ref[...] = acc

    return pl.pallas_call(body, name="sum_devices_%d" % g.shape[1],
                          out_shape=jax.ShapeDtypeStruct(g.shape[1:], F32))(g)


def _sum_windows(g, name):
    n, r, c = g.shape
    tr = 128

    def body(g_ref, o_ref):
        x, y, cc = _place()
        lane0 = (W_IN_SHARD * _slot((x, y, cc))) % DH
        acc = g_ref[0].astype(F32)
        for i in range(1, n):
            acc = acc + g_ref[i].astype(F32)
        o_ref[...] = pltpu.roll(acc, (c - lane0) % c, 1).T

    return pl.pallas_call(
        body, name=name, grid=(r // tr,),
        in_specs=[pl.BlockSpec((n, tr, c), lambda i: (0, i, 0))],
        out_specs=pl.BlockSpec((c, tr), lambda i: (0, i)),
        out_shape=jax.ShapeDtypeStruct((c, r), F32),
        compiler_params=_cp("parallel"),
    )(g)


def _adam_rows(r, c, n):
    budget = 6 * 1024 * 1024
    best = None
    for tr in range(16, r + 1, 16):
        if r % tr == 0 and tr * c * (2 * n + 28) <= budget:
            best = tr
    return best if best is not None else r


def _adamw(g, w, m, v, name):
    n, r, c = g.shape
    tr = _adam_rows(r, c, n)
    bc1 = 1.0 - ADAM_B1 ** ADAM_STEP
    bc2 = 1.0 - ADAM_B2 ** ADAM_STEP

    def body(g_ref, w_ref, m_ref, v_ref, go_ref, d_ref, mo_ref, vo_ref):
        grad = g_ref[0].astype(F32)
        for i in range(1, n):
            grad = grad + g_ref[i].astype(F32)
        go_ref[...] = grad
        m_new = ADAM_B1 * m_ref[...] + (1.0 - ADAM_B1) * grad
        v_new = ADAM_B2 * v_ref[...] + (1.0 - ADAM_B2) * (grad * grad)
        mo_ref[...] = m_new
        vo_ref[...] = v_new
        d_ref[...] = -ADAM_LR * ((m_new / bc1) / (jnp.sqrt(v_new / bc2) + ADAM_EPS) + ADAM_WD * w_ref[...])

    blk = pl.BlockSpec((tr, c), lambda i: (i, 0))
    out = jax.ShapeDtypeStruct((r, c), F32)
    return pl.pallas_call(
        body, name=name, grid=(r // tr,),
        in_specs=[pl.BlockSpec((n, tr, c), lambda i: (0, i, 0)), blk, blk, blk],
        out_specs=[blk] * 4, out_shape=[out] * 4,
        compiler_params=_cp("parallel"),
    )(g, w, m, v)


def kernel(x, c, ctx, c_ctx, w_mod, b_mod, norm_pre1, norm_post1, norm_pre2, norm_post2, w_in, hg_lb, hg_onorm, gla_w_gk, gla_b_gk, gla_onorm, w_br_hg, w_br_gla, w_out, w_ff_gate, w_ff_up, w_ff_down, loss_target, m_c_ctx, m_w_mod, m_b_mod, m_norm_pre1, m_norm_post1, m_norm_pre2, m_norm_post2, m_w_in, m_hg_lb, m_hg_onorm, m_gla_w_gk, m_gla_b_gk, m_gla_onorm, m_w_br_hg, m_w_br_gla, m_w_out, m_w_ff_gate, m_w_ff_up, m_w_ff_down, v_c_ctx, v_w_mod, v_b_mod, v_norm_pre1, v_norm_post1, v_norm_pre2, v_norm_post2, v_w_in, v_hg_lb, v_hg_onorm, v_gla_w_gk, v_gla_b_gk, v_gla_onorm, v_w_br_hg, v_w_br_gla, v_w_out, v_w_ff_gate, v_w_ff_up, v_w_ff_down):
    xi, yi, ci = lax.axis_index("x"), lax.axis_index("y"), lax.axis_index("c")
    me = 4 * xi + 2 * yi + ci
    t = CTX + x.shape[1]

    c_all, lb_g, wgk_g, bgk_g = _all_gather([c, hg_lb, gla_w_gk[0], gla_b_gk[0]], "ag_small")
    tr_ = lambda a: jnp.swapaxes(a[0], 0, 1)
    big = [w_in[0], w_br_hg[0], w_br_gla[0], w_out[0], tr_(w_ff_gate), tr_(w_ff_up), w_ff_down[0]]
    big_bf = [w.astype(BF16) for w in big]
    g_in, = _all_gather(big_bf[:1], "ag_w_in")
    gathered = lambda arrs: [(N_DEV,) + a.shape for a in arrs]
    mix_handle, tok = _split_start(_view_whole, gathered(big_bf[1:4]), big_bf[1:4], "ag_mix_start", g_in)
    ffn_handle, tok = _split_start(_view_whole, gathered(big_bf[4:]), big_bf[4:], "ag_ffn_start", tok)
    cols = lambda g: jnp.transpose(g, (1, 0, 2)).reshape(g.shape[1], N_DEV * g.shape[2])
    w_in_k = _tie(_layout_w_in(cols(g_in)), tok, "tie_w_in")

    def get_mix(after):
        g_brh, g_brg, g_out = _split_wait(mix_handle, "ag_mix_wait", after)
        return _gate_cols(cols(g_brh)), _gate_cols(cols(g_brg)), _gate_rows(g_out.reshape(D, D))

    def get_ffn(after):
        g_gate, g_up, g_down = _split_wait(ffn_handle, "ag_ffn_wait", after)
        return (g_gate.reshape(D_FF, D), g_up.reshape(D_FF, D)), g_down.reshape(D_FF, D)

    hg_lb_full = jnp.transpose(lb_g, (1, 2, 0, 3)).reshape(2, 2, HW)
    wgk_k = _layout_wgk(jnp.transpose(wgk_g, (1, 2, 0, 3)).reshape(2, 16, HW)).astype(BF16)
    bgk_k = jnp.transpose(bgk_g, (1, 0, 2)).reshape(1, D)
    onw = jnp.concatenate([jnp.tile(hg_onorm, (1, NH // 2)), jnp.tile(gla_onorm, (1, NH // 2))], axis=1)

    n_mod = w_mod.shape[2]
    a9 = jnp.concatenate([c_ctx[None], c_all[:, 0], jnp.zeros((16 - 1 - N_DEV, D), F32)], axis=0)
    b_loc = lax.dynamic_slice(b_mod, (0, me * n_mod), (1, n_mod))
    s_loc = _mod_fwd(a9, w_mod[0], b_loc)
    s_all, = _all_gather([s_loc], "ag_mod")
    mod_all = jnp.transpose(s_all, (1, 0, 2)).reshape(16, N_DEV * n_mod)
    pad8 = lambda m: jnp.concatenate([m.reshape(6, D), jnp.zeros((2, D), F32)], axis=0)
    modc = pad8(mod_all[0])
    modx = pad8(lax.dynamic_slice(mod_all, (1 + me, 0), (1, N_DEV * n_mod))[0])

    z = jnp.concatenate([ctx[0], x[0]], axis=0)
    norms = (norm_pre1, norm_post1, norm_pre2, norm_post2)
    shard = lambda d: jnp.transpose(d.reshape(d.shape[0], N_DEV, -1), (1, 0, 2)).astype(BF16)
    rowshard = lambda d: d.reshape(N_DEV, d.shape[0] // N_DEV, d.shape[1]).astype(BF16)
    sent, w_in_grad = [], {}

    def send_w_in(i, x_after):
        half, rows = W_IN_GRAD_CHUNKS[i]
        handle, tok = _split_start(_view_window(rows), [(N_DEV, rows[1] - rows[0], D)], w_in_grad[half],
                                   "grads_w_in%d_start" % i, x_after)
        w_in_grad[half] = handle["srcs"]
        sent.append(("w_in%d" % i, ["w_in#%d" % i], handle))
        return _tie(x_after, tok, "tie_w_in%d" % i)

    def send(names, grads, x_after):
        if names == ("w_in_a",):
            w_in_grad["a"] = list(grads)
            return send_w_in(0, x_after)
        if names == ("w_in_b",):
            w_in_grad["b"] = list(grads)
            return x_after
        arrs, leaves = [], []
        for nm, g in zip(names, grads):
            if nm in ("w_gate_t", "w_up_t"):
                arrs.append(rowshard(g))
                leaves.append({"w_gate_t": "w_ff_gate", "w_up_t": "w_ff_up"}[nm])
            elif nm == "w_down":
                arrs.append(rowshard(g))
                leaves.append("w_ff_down")
            elif nm == "w_out":
                arrs.append(rowshard(g[GOFF:GOFF + D]))
                leaves.append(nm)
            else:
                arrs.append(shard(g[:, GOFF:GOFF + D]))
                leaves.append(nm)
        handle, tok = _split_start(_view_block, [a.shape for a in arrs], arrs, "grads_%s_start" % names[0], x_after)
        sent.append((names[0], leaves, handle))
        return _tie(x_after, tok, "tie_" + names[0])

    r = _local_step(z, loss_target[0], modc, modx, norms, onw, hg_lb_full, wgk_k, bgk_k,
                    w_in_k, get_mix, get_ffn, send)
    grad_x = r["grad_x"][None]

    sm_pre, sm_mid, sm_fin = r["sm_pre"], r["sm_mid"], r["sm_final"]
    dmodc = jnp.stack([sm_pre[0], sm_pre[2], sm_mid[4], sm_mid[0], sm_mid[2], sm_fin[0]]).reshape(-1)
    dmodx = jnp.stack([sm_pre[1], sm_pre[3], sm_mid[5], sm_mid[1], sm_mid[3], sm_fin[1]]).reshape(-1)
    on = r["sm_post"][0].reshape(NH, DH)
    pieces = [dmodc, dmodx, sm_pre[4], sm_mid[7], sm_mid[6], sm_fin[2], on[:NH // 2].sum(0), on[NH // 2:].sum(0),
              r["d_lb"][:2].reshape(-1), _unlayout_wgk(r["d_wgk"]).reshape(-1), r["d_bgk"][0]]
    loss_local = (0.5 / D) * jnp.sum(r["loss_vec"])
    pieces.append(jnp.concatenate([loss_local.reshape(1), jnp.zeros((DH - 1,), F32)]))
    sizes = [p.shape[0] for p in pieces]
    pack = jnp.concatenate(pieces).reshape(-1, DH)
    pack_all, = _all_gather([pack], "ag_small_grads")
    pack_all = send_w_in(1, pack_all)
    tot = _sum_devices(pack_all).reshape(-1)
    offs = [sum(sizes[:i]) for i in range(len(sizes))]
    part = lambda i: tot[offs[i]:offs[i] + sizes[i]]
    dmodc_t, dmodx_t = part(0), part(1)
    g_b_mod = (dmodc_t + dmodx_t)[None]
    g_norms = [part(i)[None] for i in (2, 3, 4, 5)]
    g_hg_on, g_gla_on = part(6)[None], part(7)[None]
    lb0 = lax.dynamic_slice(part(8).reshape(2, HW), (0, me * (HW // N_DEV)), (2, HW // N_DEV))
    g_hg_lb = jnp.stack([lb0, -lb0])
    g_wgk = lax.dynamic_slice(part(9).reshape(2, 16, HW), (0, 0, me * (HW // N_DEV)), (2, 16, HW // N_DEV))[None]
    g_bgk = lax.dynamic_slice(part(10).reshape(2, HW), (0, me * (HW // N_DEV)), (2, HW // N_DEV))[None]
    loss = part(11)[0]

    dmx_all = pack_all.reshape(N_DEV, -1)[:, sizes[0]:sizes[0] + sizes[1]]
    d9 = jnp.concatenate([lax.dynamic_slice(dmodc_t[None], (0, me * n_mod), (1, n_mod)),
                          lax.dynamic_slice(dmx_all, (0, me * n_mod), (N_DEV, n_mod)),
                          jnp.zeros((16 - 1 - N_DEV, n_mod), F32)], axis=0)
    g_w_mod, dcc_part = _mod_bwd(a9, d9, w_mod[0])
    dcc_all, = _all_gather([dcc_part], "ag_c_ctx")
    dcc_all = send_w_in(2, dcc_all)
    g_c_ctx = _sum_devices(dcc_all)[0]

    recv = {}
    for first, leaves, handle in sent:
        if not first.startswith("w_in"):
            recv.update(zip(leaves, _split_wait(handle, "grads_%s_wait" % first, g_c_ctx)))
    moms = [(m_w_in, v_w_in), (m_w_br_hg, v_w_br_hg), (m_w_br_gla, v_w_br_gla), (m_w_out, v_w_out),
            (m_w_ff_gate, v_w_ff_gate), (m_w_ff_up, v_w_ff_up), (m_w_ff_down, v_w_ff_down)]
    names = ["w_in", "w_br_hg", "w_br_gla", "w_out", "w_ff_gate", "w_ff_up", "w_ff_down"]
    res = {}

    def update(nm, w, m, v):
        if nm in ("w_ff_gate", "w_ff_up"):
            outs = _adamw(recv[nm], w, tr_(m), tr_(v), "adamw_" + nm)
            res[nm] = [jnp.swapaxes(o, 0, 1)[None] for o in outs]
        else:
            res[nm] = [o[None] for o in _adamw(recv[nm], w, m[0], v[0], "adamw_" + nm)]

    for nm, w, (m, v) in list(zip(names, big, moms))[1:]:
        update(nm, w, m, v)
    res["w_mod"] = [o[None] for o in _adamw(g_w_mod[None], w_mod[0], m_w_mod[0], v_w_mod[0], "adamw_w_mod")]

    small = [("c_ctx", c_ctx, m_c_ctx, v_c_ctx, g_c_ctx), ("b_mod", b_mod, m_b_mod, v_b_mod, g_b_mod),
             ("norm_pre1", norm_pre1, m_norm_pre1, v_norm_pre1, g_norms[0]),
             ("norm_post1", norm_post1, m_norm_post1, v_norm_post1, g_norms[1]),
             ("norm_pre2", norm_pre2, m_norm_pre2, v_norm_pre2, g_norms[2]),
             ("norm_post2", norm_post2, m_norm_post2, v_norm_post2, g_norms[3]),
             ("hg_lb", hg_lb, m_hg_lb, v_hg_lb, g_hg_lb), ("hg_onorm", hg_onorm, m_hg_onorm, v_hg_onorm, g_hg_on),
             ("gla_w_gk", gla_w_gk, m_gla_w_gk, v_gla_w_gk, g_wgk), ("gla_b_gk", gla_b_gk, m_gla_b_gk, v_gla_b_gk, g_bgk),
             ("gla_onorm", gla_onorm, m_gla_onorm, v_gla_onorm, g_gla_on)]
    flat = lambda k: jnp.concatenate([s[k].reshape(-1) for s in small]).reshape(-1, DH)
    outs = _adamw(flat(4)[None], flat(1), flat(2), flat(3), "adamw_small")
    off = 0
    for nm, w, _, _, _ in small:
        res[nm] = [o.reshape(-1)[off:off + w.size].reshape(w.shape) for o in outs]
        off += w.size

    done = [res[nm][0] for nm in names[1:]] + [res["w_mod"][0], outs[0]]
    sums = []
    for i, (first, leaves, handle) in enumerate(s for s in sent if s[0].startswith("w_in")):
        half = W_IN_GRAD_CHUNKS[i][0]
        land, = _split_wait(handle, "grads_%s_wait" % first, done, srcs=w_in_grad[half])
        w_in_grad[half] = handle["srcs"]
        sums.append(_sum_windows(land, "sum_windows%d" % i))
    g_t = jnp.concatenate(sums, axis=1)[:W_IN_SHARD]
    lin = lambda a: a.reshape(W_IN_SHARD * D // DH, DH)
    outs = _adamw(lin(g_t)[None], lin(tr_(w_in)), lin(tr_(m_w_in)), lin(tr_(v_w_in)), "adamw_w_in")
    res["w_in"] = [jnp.swapaxes(o.reshape(W_IN_SHARD, D), 0, 1)[None] for o in outs]

    order = ["c_ctx", "w_mod", "b_mod", "norm_pre1", "norm_post1", "norm_pre2", "norm_post2", "w_in", "hg_lb",
             "hg_onorm", "gla_w_gk", "gla_b_gk", "gla_onorm", "w_br_hg", "w_br_gla", "w_out", "w_ff_gate", "w_ff_up",
             "w_ff_down"]
    return (loss, grad_x, *[res[n][k] for k in range(4) for n in order])
```

```python
import functools

import jax
import jax.numpy as jnp
from jax import lax
from jax.experimental import pallas as pl
from jax.experimental.pallas import tpu as pltpu

F32 = jnp.float32
BF16 = jnp.bfloat16
HI = lax.Precision.HIGHEST

N_DEV = 8
D = 1024
CTX = 256
HW = 512
DH = 128
NH = 8
D_FF = 2816
EPS = 1e-6
GLA_NORM = 16.0
CHUNK = 64
TR = 256
NCT = CTX // TR
W_IN_COLS = 7168
MAIN0 = 0
LR0 = 4608
GW = 1152
GOFF = 32
GATE_HG0 = LR0
GATE_GLA0 = LR0 + D
LEVELS = (32, 16, 8)
EXP_CLAMP = 80.0
VMEM_LIMIT = 48 * 1024 * 1024

ADAM_LR, ADAM_B1, ADAM_B2, ADAM_EPS, ADAM_WD, ADAM_STEP = 0.001, 0.9, 0.999, 1e-08, 0.01, 10


def _cp(*sem):
    return pltpu.CompilerParams(dimension_semantics=sem, vmem_limit_bytes=VMEM_LIMIT)


def _sig(x):
    return jax.nn.sigmoid(x)


def _silu(x):
    return x * _sig(x)


def _dsilu(x):
    s = _sig(x)
    return s * (1.0 + x * (1.0 - s))


def _rstd(x):
    return lax.rsqrt(jnp.mean(x * x, axis=-1, keepdims=True) + EPS)


def _rms_bwd(a, y, r):
    return r * (a - y * (r * r) * jnp.mean(a * y, axis=-1, keepdims=True))


def _colsum(x):
    return jnp.sum(x, axis=0, keepdims=True)


def _dot(a, b, dims, precision=None):
    return lax.dot_general(a, b, (dims, ((), ())), preferred_element_type=F32, precision=precision)


NN = ((1,), (0,))
NT = ((1,), (1,))
TN = ((0,), (0,))

SCAN_HEADS_FWD = 4
SCAN_HEADS_BWD = 4


def _split_dot(m, x):
    mb = m.astype(BF16)
    x1 = x.astype(BF16)
    r1 = x - x1.astype(F32)
    x2 = r1.astype(BF16)
    x3 = (r1 - x2.astype(F32)).astype(BF16)
    return _dot(mb, x1, NN) + _dot(mb, x2, NN) + _dot(mb, x3, NN)


def _matmul(a, b, dims, out_dtype, name, tm, tn, tk, a_off=0, m_out=None):
    pair = isinstance(b, (tuple, list))
    bs = list(b) if pair else [b]
    b1 = bs[0]
    rows = b1.shape[0] * len(bs)
    half = None
    if dims == NN:
        m, k, n = a.shape[0], rows, b1.shape[1]
        a_spec = pl.BlockSpec((tm, tk), lambda i, j, kk: (i, kk + a_off))
        half = b1.shape[0] // tk
        b_maps = [lambda i, j, kk: (kk, j)] if not pair else [
            lambda i, j, kk: (jnp.minimum(kk, half - 1), j), lambda i, j, kk: (jnp.maximum(kk - half, 0), j)]
        b_specs = [pl.BlockSpec((tk, tn), f) for f in b_maps]
        axis = 2
    elif dims == NT:
        m, k, n = a.shape[0], b1.shape[1], rows
        a_spec = pl.BlockSpec((tm, tk), lambda i, j, kk: (i, kk + a_off))
        half = b1.shape[0] // tn
        b_maps = [lambda i, j, kk: (j, kk)] if not pair else [
            lambda i, j, kk: (jnp.minimum(j, half - 1), kk), lambda i, j, kk: (jnp.maximum(j - half, 0), kk)]
        b_specs = [pl.BlockSpec((tn, tk), f) for f in b_maps]
        axis = 1
    else:
        assert not pair
        m, k = (a.shape[1] if m_out is None else m_out), a.shape[0]
        n = b1.shape[1]
        a_spec = pl.BlockSpec((tk, tm), lambda i, j, kk: (kk, i + a_off))
        b_specs = [pl.BlockSpec((tk, tn), lambda i, j, kk: (kk, j))]
    assert m % tm == 0 and n % tn == 0 and k % tk == 0, (name, m, n, k, tm, tn, tk)
    nk = k // tk
    nb = len(bs)

    def body(a_ref, *refs):
        o_ref = refs[nb]
        if pair:
            bv = jnp.where(pl.program_id(axis) < half, refs[0][...], refs[1][...])
        else:
            bv = refs[0][...]
        part = _dot(a_ref[...], bv, dims)
        if nk == 1:
            o_ref[...] = part.astype(o_ref.dtype)
            return
        acc_ref = refs[nb + 1]
        kk = pl.program_id(2)

        @pl.when(kk == 0)
        def _():
            acc_ref[...] = part

        @pl.when(kk > 0)
        def _():
            acc_ref[...] += part

        @pl.when(kk == nk - 1)
        def _():
            o_ref[...] = acc_ref[...].astype(o_ref.dtype)

    return pl.pallas_call(
        body,
        name=name,
        grid=(m // tm, n // tn, nk),
        in_specs=[a_spec] + b_specs,
        out_specs=pl.BlockSpec((tm, tn), lambda i, j, kk: (i, j)),
        out_shape=jax.ShapeDtypeStruct((m, n), out_dtype),
        scratch_shapes=[] if nk == 1 else [pltpu.VMEM((tm, tn), F32)],
        compiler_params=_cp("parallel", "parallel", "arbitrary"),
    )(a, *bs)


def _row(c):
    return pl.BlockSpec((TR, c), lambda i: (i, 0))


def _rowcol(width, cb):
    return pl.BlockSpec((TR, width), lambda i: (i, cb))


def _full(shape):
    return pl.BlockSpec(shape, lambda i: (0,) * len(shape))


def _mod_row(mc_ref, mx_ref, k, is_ctx):
    return jnp.where(is_ctx, mc_ref[k:k + 1, :], mx_ref[k:k + 1, :])


def _acc_row(ref, k, val):
    ref[k:k + 1, :] += val


def _acc_mod(ref, k, is_ctx, val):
    zero = jnp.zeros_like(val)
    ref[k:k + 1, :] += jnp.where(is_ctx, val, zero)
    ref[k + 1:k + 2, :] += jnp.where(is_ctx, zero, val)


def _prenorm(z, nw, modc, modx, i_shift, i_scale, name):
    t = z.shape[0]

    def body(z_ref, nw_ref, mc_ref, mx_ref, h_ref):
        is_ctx = pl.program_id(0) < NCT
        x = z_ref[...]
        n = x * _rstd(x) * nw_ref[...]
        h = n * (1.0 + _mod_row(mc_ref, mx_ref, i_scale, is_ctx)) + _mod_row(mc_ref, mx_ref, i_shift, is_ctx)
        h_ref[...] = h.astype(BF16)

    return pl.pallas_call(
        body, name=name, grid=(t // TR,),
        in_specs=[_row(D), _full((1, D)), _full((8, D)), _full((8, D))],
        out_specs=_row(D),
        out_shape=jax.ShapeDtypeStruct((t, D), BF16),
        compiler_params=_cp("parallel"),
    )(z, nw, modc, modx)


def _hg_lb(lb_ref, d):
    a0 = lb_ref[0, d:d + 1, :]
    a1 = lb_ref[1, d:d + 1, :]
    mx = jnp.maximum(a0, a1)
    e0 = jnp.exp(a0 - mx)
    e1 = jnp.exp(a1 - mx)
    return e0 / (e0 + e1)


def _log_sigmoid(x):
    return jnp.minimum(x, 0.0) - jnp.log(1.0 + jnp.exp(-jnp.abs(x)))


def _gates_fwd(p, hg_lb, wgk, bgk):
    t = p.shape[0]
    seg = lambda j: _rowcol(HW, MAIN0 // HW + j)

    def body(hq_ref, hi_ref, hf_ref, hb_ref, gq_ref, gk_ref, gv_ref, lr_ref, lb_ref, wgk_ref, bgk_ref,
             q_ref, v_ref, kf_ref, kb_ref, gf_ref, gb_ref):
        q_ref[:, :HW] = _silu(hq_ref[...].astype(F32)).astype(BF16)
        q_ref[:, HW:] = (gq_ref[...].astype(F32) * (DH ** -0.5)).astype(BF16)
        v_ref[:, :HW] = hi_ref[...]
        v_ref[:, HW:] = gv_ref[...]
        xg = _dot(lr_ref[...].astype(BF16), wgk_ref[...], NN) + bgk_ref[...]
        for d, (raw_ref, k_ref, g_ref) in enumerate(((hf_ref, kf_ref, gf_ref), (hb_ref, kb_ref, gb_ref))):
            lbd = _hg_lb(lb_ref, d)
            f = lbd + (1.0 - lbd) * _sig(raw_ref[...].astype(F32))
            k_ref[:, :HW] = (1.0 - f).astype(BF16)
            k_ref[:, HW:] = gk_ref[...]
            g_ref[:, :HW] = jnp.log(f)
            g_ref[:, HW:] = _log_sigmoid(xg[:, d * HW:(d + 1) * HW]) * (1.0 / GLA_NORM)

    out = jax.ShapeDtypeStruct((t, D), F32)
    outb = jax.ShapeDtypeStruct((t, D), BF16)
    return pl.pallas_call(
        body, name="gates_fwd", grid=(t // TR,),
        in_specs=[seg(0), seg(1), seg(2), seg(3), seg(5), seg(6), seg(7), _rowcol(DH, LR0 // DH),
                  _full((2, 2, HW)), _full((DH, D)), _full((1, D))],
        out_specs=[_row(D)] * 6,
        out_shape=[outb] * 4 + [out] * 2,
        compiler_params=_cp("parallel"),
    )(p, p, p, p, p, p, p, p, hg_lb, wgk, bgk)


def _post_fwd(o_fw, o_bw, p, onw):
    t = o_fw.shape[0]

    def body(of_ref, ob_ref, g1_ref, g2_ref, w_ref, y_ref):
        for h in range(NH):
            sl = slice(h * DH, (h + 1) * DH)
            o = of_ref[:, sl] + ob_ref[:, sl]
            g_ref = g1_ref if h < NH // 2 else g2_ref
            gs = slice((h % (NH // 2)) * DH, (h % (NH // 2) + 1) * DH)
            n = o * _rstd(o) * w_ref[:, sl]
            y_ref[:, sl] = (n * _silu(g_ref[:, gs].astype(F32))).astype(BF16)

    return pl.pallas_call(
        body, name="post_fwd", grid=(t // TR,),
        in_specs=[_row(D), _row(D), _rowcol(HW, MAIN0 // HW + 4), _rowcol(HW, MAIN0 // HW + 8), _full((1, D))],
        out_specs=_row(D),
        out_shape=jax.ShapeDtypeStruct((t, D), BF16),
        compiler_params=_cp("parallel"),
    )(o_fw, o_bw, p, p, onw)


def _gate_window_specs(col0):
    return [_rowcol(HW, col0 // HW), _rowcol(HW, col0 // HW + 1), _rowcol(DH, (col0 + 2 * HW) // DH)]


def _gate_window(refs):
    return jnp.concatenate([r[...].astype(F32) for r in refs], axis=1)


def _merge_fwd(p, u1, u2):
    t = p.shape[0]

    def body(a0, a1, a2, b0, b1, b2, u1_ref, u2_ref, m_ref):
        f = lambda r: r[...].astype(F32)
        m_ref[...] = (_sig(_gate_window((a0, a1, a2))) * f(u1_ref)
                      + _sig(_gate_window((b0, b1, b2))) * f(u2_ref)).astype(BF16)

    return pl.pallas_call(
        body, name="merge_fwd", grid=(t // TR,),
        in_specs=_gate_window_specs(GATE_HG0) + _gate_window_specs(GATE_GLA0) + [_row(GW), _row(GW)],
        out_specs=_row(GW),
        out_shape=jax.ShapeDtypeStruct((t, GW), BF16),
        compiler_params=_cp("parallel"),
    )(p, p, p, p, p, p, u1, u2)


def _mid_fwd(z, y1, nw_post, nw_pre, modc, modx):
    t = z.shape[0]

    def body(z_ref, y_ref, wpo_ref, wpr_ref, mc_ref, mx_ref, z1_ref, h_ref):
        is_ctx = pl.program_id(0) < NCT
        y = y_ref[...].astype(F32)
        z1 = z_ref[...] + _mod_row(mc_ref, mx_ref, 2, is_ctx) * (y * _rstd(y) * wpo_ref[...])
        z1_ref[...] = z1
        n = z1 * _rstd(z1) * wpr_ref[...]
        h = n * (1.0 + _mod_row(mc_ref, mx_ref, 4, is_ctx)) + _mod_row(mc_ref, mx_ref, 3, is_ctx)
        h_ref[...] = h.astype(BF16)

    return pl.pallas_call(
        body, name="mid_fwd", grid=(t // TR,),
        in_specs=[_row(D), _row(D), _full((1, D)), _full((1, D)), _full((8, D)), _full((8, D))],
        out_specs=[_row(D), _row(D)],
        out_shape=[jax.ShapeDtypeStruct((t, D), F32), jax.ShapeDtypeStruct((t, D), BF16)],
        compiler_params=_cp("parallel"),
    )(z, y1, nw_post, nw_pre, modc, modx)


def _swiglu_fwd(uv):
    t = uv.shape[0]

    def body(u_ref, v_ref, a_ref):
        a_ref[...] = (_silu(u_ref[...].astype(F32)) * v_ref[...].astype(F32)).astype(BF16)

    return pl.pallas_call(
        body, name="swiglu_fwd", grid=(t // TR,),
        in_specs=[_rowcol(D_FF, 0), _rowcol(D_FF, 1)],
        out_specs=_row(D_FF),
        out_shape=jax.ShapeDtypeStruct((t, D_FF), BF16),
        compiler_params=_cp("parallel"),
    )(uv, uv)


def _swiglu_bwd(uv, da):
    t = uv.shape[0]

    def body(u_ref, v_ref, da_ref, d_ref):
        u = u_ref[...].astype(F32)
        d = da_ref[...].astype(F32)
        d_ref[:, :D_FF] = (d * v_ref[...].astype(F32) * _dsilu(u)).astype(BF16)
        d_ref[:, D_FF:] = (d * _silu(u)).astype(BF16)

    return pl.pallas_call(
        body, name="swiglu_bwd", grid=(t // TR,),
        in_specs=[_rowcol(D_FF, 0), _rowcol(D_FF, 1), _row(D_FF)],
        out_specs=_row(2 * D_FF),
        out_shape=jax.ShapeDtypeStruct((t, 2 * D_FF), BF16),
        compiler_params=_cp("parallel"),
    )(uv, uv, da)


def _final(z1, y2, target, nw, modc, modx):
    t = z1.shape[0]

    def body(z1_ref, y_ref, tg_ref, w_ref, mc_ref, mx_ref, dz_ref, dy_ref, loss_ref, sm_ref):
        i = pl.program_id(0)
        is_ctx = i < NCT

        @pl.when(i == 0)
        def _():
            loss_ref[...] = jnp.zeros_like(loss_ref)
            sm_ref[...] = jnp.zeros_like(sm_ref)

        g = _mod_row(mc_ref, mx_ref, 5, is_ctx)
        y = y_ref[...].astype(F32)
        r = _rstd(y)
        w = w_ref[...]
        yr = y * r
        n = yr * w
        e = z1_ref[...] + g * n - tg_ref[...]
        lat = jnp.where(is_ctx, 0.0, 1.0)
        loss_ref[...] += lat * _colsum(e * e)
        dz = e * (lat / D)
        dz_ref[...] = dz
        _acc_mod(sm_ref, 0, is_ctx, _colsum(dz * n))
        dn = dz * g
        _acc_row(sm_ref, 2, _colsum(dn * yr))
        dy_ref[...] = _rms_bwd(dn * w, y, r).astype(BF16)

    return pl.pallas_call(
        body, name="final", grid=(t // TR,),
        in_specs=[_row(D), _row(D), pl.BlockSpec((TR, D), lambda i: (jnp.maximum(i - NCT, 0), 0)),
                  _full((1, D)), _full((8, D)), _full((8, D))],
        out_specs=[_row(D), _row(D), _full((1, D)), _full((8, D))],
        out_shape=[jax.ShapeDtypeStruct((t, D), F32), jax.ShapeDtypeStruct((t, D), BF16),
                   jax.ShapeDtypeStruct((1, D), F32), jax.ShapeDtypeStruct((8, D), F32)],
        compiler_params=_cp("arbitrary"),
    )(z1, y2, target, nw, modc, modx)


def _mid_bwd(dh2, dz, z, z1, y1, nw_post, nw_pre, modc, modx):
    t = z.shape[0]

    def body(dh_ref, dz_ref, z_ref, z1_ref, y_ref, wpo_ref, wpr_ref, mc_ref, mx_ref, dzo_ref, dy_ref, sm_ref):
        i = pl.program_id(0)
        is_ctx = i < NCT

        @pl.when(i == 0)
        def _():
            sm_ref[...] = jnp.zeros_like(sm_ref)

        dh = dh_ref[...].astype(F32)
        z1 = z1_ref[...]
        r = _rstd(z1)
        zr = z1 * r
        wpr = wpr_ref[...]
        n = zr * wpr
        _acc_mod(sm_ref, 0, is_ctx, _colsum(dh))
        _acc_mod(sm_ref, 2, is_ctx, _colsum(dh * n))
        dn = dh * (1.0 + _mod_row(mc_ref, mx_ref, 4, is_ctx))
        _acc_row(sm_ref, 6, _colsum(dn * zr))
        dz1 = dz_ref[...] + _rms_bwd(dn * wpr, z1, r)
        dzo_ref[...] = dz1
        y = y_ref[...].astype(F32)
        r1 = _rstd(y)
        yr = y * r1
        wpo = wpo_ref[...]
        g = _mod_row(mc_ref, mx_ref, 2, is_ctx)
        _acc_mod(sm_ref, 4, is_ctx, _colsum(dz1 * (yr * wpo)))
        dn1 = dz1 * g
        _acc_row(sm_ref, 7, _colsum(dn1 * yr))
        dy_ref[...] = _rms_bwd(dn1 * wpo, y, r1).astype(BF16)

    return pl.pallas_call(
        body, name="mid_bwd", grid=(t // TR,),
        in_specs=[_row(D)] * 5 + [_full((1, D)), _full((1, D)), _full((8, D)), _full((8, D))],
        out_specs=[_row(D), _row(D), _full((8, D))],
        out_shape=[jax.ShapeDtypeStruct((t, D), F32), jax.ShapeDtypeStruct((t, D), BF16),
                   jax.ShapeDtypeStruct((8, D), F32)],
        compiler_params=_cp("arbitrary"),
    )(dh2, dz, z, z1, y1, nw_post, nw_pre, modc, modx)


def _pre_bwd(dh1, dz, z, nw, modc, modx):
    t = z.shape[0]

    def body(dh_ref, dz_ref, z_ref, w_ref, mc_ref, mx_ref, dzo_ref, sm_ref):
        i = pl.program_id(0)
        is_ctx = i < NCT

        @pl.when(i == 0)
        def _():
            sm_ref[...] = jnp.zeros_like(sm_ref)

        dh = dh_ref[...].astype(F32)
        x = z_ref[...]
        r = _rstd(x)
        xr = x * r
        w = w_ref[...]
        _acc_mod(sm_ref, 0, is_ctx, _colsum(dh))
        _acc_mod(sm_ref, 2, is_ctx, _colsum(dh * (xr * w)))
        dn = dh * (1.0 + _mod_row(mc_ref, mx_ref, 1, is_ctx))
        _acc_row(sm_ref, 4, _colsum(dn * xr))
        dzo_ref[...] = dz_ref[...] + _rms_bwd(dn * w, x, r)

    return pl.pallas_call(
        body, name="pre_bwd", grid=(t // TR,),
        in_specs=[_row(D)] * 3 + [_full((1, D)), _full((8, D)), _full((8, D))],
        out_specs=[pl.BlockSpec((TR, D), lambda i: (jnp.maximum(i - NCT, 0), 0)), _full((8, D))],
        out_shape=[jax.ShapeDtypeStruct((t - CTX, D), F32), jax.ShapeDtypeStruct((8, D), F32)],
        compiler_params=_cp("arbitrary"),
    )(dh1, dz, z, nw, modc, modx)


def _merge_bwd(dm, p, u1, u2):
    t = dm.shape[0]

    def body(dm_ref, a0, a1, a2, b0, b1, b2, u1_ref, u2_ref, du1_ref, du2_ref, dg_ref):
        dm_ = dm_ref[...].astype(F32)
        s1 = _sig(_gate_window((a0, a1, a2)))
        s2 = _sig(_gate_window((b0, b1, b2)))
        du1_ref[...] = (dm_ * s1).astype(BF16)
        du2_ref[...] = (dm_ * s2).astype(BF16)
        dg_ref[:, :GW] = (dm_ * u1_ref[...].astype(F32) * s1 * (1.0 - s1)).astype(BF16)
        dg_ref[:, GW:] = (dm_ * u2_ref[...].astype(F32) * s2 * (1.0 - s2)).astype(BF16)

    return pl.pallas_call(
        body, name="merge_bwd", grid=(t // TR,),
        in_specs=[_row(GW)] + _gate_window_specs(GATE_HG0) + _gate_window_specs(GATE_GLA0) + [_row(GW), _row(GW)],
        out_specs=[_row(GW), _row(GW), _row(2 * GW)],
        out_shape=[jax.ShapeDtypeStruct((t, GW), BF16), jax.ShapeDtypeStruct((t, GW), BF16),
                   jax.ShapeDtypeStruct((t, 2 * GW), BF16)],
        compiler_params=_cp("parallel"),
    )(dm, p, p, p, p, p, p, u1, u2)


def _post_bwd(dy_hg, dy_gla, o_fw, o_bw, p, onw):
    t = o_fw.shape[0]

    def body(d1_ref, d2_ref, of_ref, ob_ref, g1_ref, g2_ref, w_ref, do_ref, dg_ref, sm_ref):
        @pl.when(pl.program_id(0) == 0)
        def _():
            sm_ref[...] = jnp.zeros_like(sm_ref)

        for h in range(NH):
            sl = slice(h * DH, (h + 1) * DH)
            gs = slice((h % (NH // 2)) * DH, (h % (NH // 2) + 1) * DH)
            g_ref, d_ref = (g1_ref, d1_ref) if h < NH // 2 else (g2_ref, d2_ref)
            o = of_ref[:, sl] + ob_ref[:, sl]
            r = _rstd(o)
            orr = o * r
            w = w_ref[:, sl]
            gt = g_ref[:, gs].astype(F32)
            dy = d_ref[:, gs].astype(F32)
            dg_ref[:, sl] = (dy * (orr * w) * _dsilu(gt)).astype(BF16)
            dn = dy * _silu(gt)
            sm_ref[0:1, sl] += _colsum(dn * orr)
            do_ref[:, sl] = _rms_bwd(dn * w, o, r)

    return pl.pallas_call(
        body, name="post_bwd", grid=(t // TR,),
        in_specs=[_row(HW), _row(HW), _row(D), _row(D), _rowcol(HW, MAIN0 // HW + 4), _rowcol(HW, MAIN0 // HW + 8),
                  _full((1, D))],
        out_specs=[_row(D), _row(D), _full((8, D))],
        out_shape=[jax.ShapeDtypeStruct((t, D), F32), jax.ShapeDtypeStruct((t, D), BF16),
                   jax.ShapeDtypeStruct((8, D), F32)],
        compiler_params=_cp("arbitrary"),
    )(dy_hg, dy_gla, o_fw, o_bw, p, p, onw)


def _gates_bwd(p, hg_lb, wgk, bgk, dgm, dgo, dq_f, dq_b, dv_f, dv_b, dk_f, dk_b, dg_f, dg_b):
    t = p.shape[0]
    seg = lambda j: _rowcol(HW, MAIN0 // HW + j)

    def body(hq_ref, hf_ref, hb_ref, lr_ref, lb_ref, wgk_ref, bgk_ref, dgm_ref, dgo_ref,
             dqf_ref, dqb_ref, dvf_ref, dvb_ref, dkf_ref, dkb_ref, dgf_ref, dgb_ref,
             dp_ref, dlb_ref, dw_ref, db_ref):
        @pl.when(pl.program_id(0) == 0)
        def _():
            dlb_ref[...] = jnp.zeros_like(dlb_ref)
            dw_ref[...] = jnp.zeros_like(dw_ref)
            db_ref[...] = jnp.zeros_like(db_ref)

        c0 = MAIN0

        def put(j, val):
            dp_ref[:, c0 + j * HW:c0 + (j + 1) * HW] = val.astype(BF16)

        dq = dqf_ref[...].astype(F32) + dqb_ref[...].astype(F32)
        dv = dvf_ref[...].astype(F32) + dvb_ref[...].astype(F32)
        put(0, dq[:, :HW] * _dsilu(hq_ref[...].astype(F32)))
        put(1, dv[:, :HW])
        put(5, dq[:, HW:] * (DH ** -0.5))
        put(7, dv[:, HW:])
        put(6, dkf_ref[:, HW:].astype(F32) + dkb_ref[:, HW:].astype(F32))
        dp_ref[:, c0 + 4 * HW:c0 + 5 * HW] = dgo_ref[:, :HW]
        dp_ref[:, c0 + 8 * HW:c0 + 9 * HW] = dgo_ref[:, HW:]
        lr = lr_ref[...].astype(BF16)
        xg = _dot(lr, wgk_ref[...], NN) + bgk_ref[...]
        dxg = []
        for d, (raw_ref, dk_ref, dg_ref) in enumerate(((hf_ref, dkf_ref, dgf_ref), (hb_ref, dkb_ref, dgb_ref))):
            lbd = _hg_lb(lb_ref, d)
            s = _sig(raw_ref[...].astype(F32))
            f = lbd + (1.0 - lbd) * s
            df = dg_ref[:, :HW] / f - dk_ref[:, :HW].astype(F32)
            put(2 + d, df * (1.0 - lbd) * s * (1.0 - s))
            dlb_ref[d:d + 1, :] += _colsum(df * (1.0 - s)) * (lbd * (1.0 - lbd))
            dxg.append(dg_ref[:, HW:] * (1.0 / GLA_NORM) * _sig(-xg[:, d * HW:(d + 1) * HW]))
        dxg = jnp.concatenate(dxg, axis=1)
        db_ref[0:1, :] += _colsum(dxg)
        dxg_b = dxg.astype(BF16)
        dw_ref[...] += _dot(lr, dxg_b, TN)
        dlr = _dot(dxg_b, wgk_ref[...], NT)
        dp_ref[:, LR0:LR0 + DH] = (dlr + dgm_ref[:, :DH].astype(F32)).astype(BF16)
        dp_ref[:, LR0 + DH:GATE_GLA0] = dgm_ref[:, DH:D]
        dp_ref[:, GATE_GLA0:GATE_GLA0 + DH] = dgm_ref[:, D:GW] + dgm_ref[:, GW:GW + DH]
        dp_ref[:, GATE_GLA0 + DH:GATE_GLA0 + GW] = dgm_ref[:, GW + DH:]
        dp_ref[:, GATE_GLA0 + GW:] = jnp.zeros((TR, W_IN_COLS - GATE_GLA0 - GW), BF16)

    return pl.pallas_call(
        body, name="gates_bwd", grid=(t // TR,),
        in_specs=[seg(0), seg(2), seg(3), _rowcol(DH, LR0 // DH), _full((2, 2, HW)), _full((DH, D)), _full((1, D)),
                  _row(2 * GW), _row(D)] + [_row(D)] * 8,
        out_specs=[_row(W_IN_COLS), _full((8, HW)), _full((DH, D)), _full((8, D))],
        out_shape=[jax.ShapeDtypeStruct((t, W_IN_COLS), BF16), jax.ShapeDtypeStruct((8, HW), F32),
                   jax.ShapeDtypeStruct((DH, D), F32), jax.ShapeDtypeStruct((8, D), F32)],
        compiler_params=_cp("arbitrary"),
    )(p, p, p, p, hg_lb, wgk, bgk, dgm, dgo, dq_f, dq_b, dv_f, dv_b, dk_f, dk_b, dg_f, dg_b)


def _scan_consts(rev):
    r = lax.broadcasted_iota(jnp.int32, (CHUNK, CHUNK), 0)
    u = lax.broadcasted_iota(jnp.int32, (CHUNK, CHUNK), 1)
    rp = lax.broadcasted_iota(jnp.int32, (CHUNK, 1), 0)
    if rev:
        r, u, rp = CHUNK - 1 - r, CHUNK - 1 - u, CHUNK - 1 - rp
    tri = jnp.where(u <= r, 1.0, 0.0).astype(F32)
    tri_t = jnp.where(r <= u, 1.0, 0.0).astype(F32)
    lv = []
    for b in LEVELS:
        sh = b.bit_length() - 1
        pair = ((r >> sh) == (u >> sh) + 1) & (((u >> sh) & 1) == 0)
        pair_t = ((u >> sh) == (r >> sh) + 1) & (((r >> sh) & 1) == 0)
        tside = ((rp >> sh) & 1) == 1
        lv.append((pair, pair_t, tside))
    bd = LEVELS[-1].bit_length() - 1
    diag = ((r >> bd) == (u >> bd)) & (u <= r)
    diag_t = ((r >> bd) == (u >> bd)) & (r <= u)
    return tri, tri_t, lv, diag, diag_t


def _row_of(pos, rev):
    return CHUNK - 1 - pos if rev else pos


def _chunk_terms(cum, b_scr, consts, rev):
    _, _, lv, _, _ = consts
    terms = []
    for b, (_, _, tside) in zip(LEVELS, lv):
        pieces = []
        for j in range(CHUNK // (2 * b)):
            row = _row_of(2 * b * j + b - 1, rev)
            pieces.append(jnp.broadcast_to(b_scr[row:row + 1, :], (2 * b, DH)))
        if rev:
            pieces = pieces[::-1]
        bnd = pieces[0] if len(pieces) == 1 else jnp.concatenate(pieces, axis=0)
        w = jnp.exp(jnp.minimum(jnp.where(tside, cum - bnd, bnd - cum), 0.0))
        wq = jnp.where(tside, w, 0.0)
        wk = jnp.where(tside, 0.0, w)
        terms.append((wq, wk))
    b = LEVELS[-1]
    pieces = []
    for j in range(CHUNK // b):
        if j == 0:
            pieces.append(jnp.zeros((b, DH), F32))
        else:
            row = _row_of(b * j - 1, rev)
            pieces.append(jnp.broadcast_to(b_scr[row:row + 1, :], (b, DH)))
    if rev:
        pieces = pieces[::-1]
    start = jnp.concatenate(pieces, axis=0)
    wq = jnp.exp(jnp.minimum(cum - start, 0.0))
    wk = jnp.exp(jnp.minimum(start - cum, EXP_CLAMP))
    terms.append((wq, wk))
    return terms


def _run_staged(units):
    live = list(units)
    while live:
        nxt = []
        for u in live:
            try:
                next(u)
                nxt.append(u)
            except StopIteration:
                pass
        live = nxt


SCAN_TB = 256
SCAN_CB = SCAN_TB // CHUNK


def _block_order(i, ntb, rev):
    nctx = CTX // SCAN_TB
    if not rev:
        return i
    return jnp.where(i < nctx, nctx - 1 - i, ntb - 1 - (i - nctx))


def _chunk_in_block(j, rev):
    return SCAN_CB - 1 - j if rev else j


def _scan_fwd(q, k, v, g, rev):
    t = q.shape[0]
    nc = t // CHUNK
    hpb = SCAN_HEADS_FWD

    def body(q_ref, k_ref, v_ref, g_ref, o_ref, st_ref, s_scr, b_scr):
        consts = _scan_consts(rev)
        _, _, lv, diag, _ = consts
        masks = [pair for pair, _, _ in lv] + [diag]

        @pl.when(pl.program_id(1) == 0)
        def _():
            s_scr[...] = jnp.zeros_like(s_scr)

        tri = consts[0]
        state = {hh: s_scr[hh] for hh in range(hpb)}

        def unit(hh, j):
            sl = slice(hh * DH, (hh + 1) * DH)
            c = _chunk_in_block(j, rev)
            rows = slice(c * CHUNK, (c + 1) * CHUNK)
            b_ref = b_scr.at[hh * SCAN_CB + j]
            qc, kc, vc, gc = q_ref[rows, sl], k_ref[rows, sl], v_ref[rows, sl], g_ref[rows, sl]
            cum = _split_dot(tri, gc)
            b_ref[...] = cum
            yield
            terms = _chunk_terms(cum, b_ref, consts, rev)
            ops = [((qc * wq).astype(BF16), (kc * wk).astype(BF16)) for wq, wk in terms]
            tot = _colsum(gc)
            qe = (qc * jnp.exp(cum)).astype(BF16)
            ke = (kc * jnp.exp(tot - cum)).astype(BF16)
            vb = vc.astype(BF16)
            yield
            scs = [_dot(qt, kt, NT) for qt, kt in ops]
            kv = _dot(vb, ke, TN)
            yield
            a = jnp.zeros((CHUNK, CHUNK), F32)
            for sc, m in zip(scs, masks):
                a = a + jnp.where(m, sc, 0.0)
            o_intra = _dot(a.astype(BF16), vb, NN)
            yield
            st = state[hh]
            st_ref[hh, c] = st
            o_ref[rows, sl] = o_intra + _dot(qe, st.astype(BF16), NT)
            state[hh] = st * jnp.exp(tot) + kv
            yield

        _run_staged([unit(hh, j) for hh in range(hpb) for j in range(SCAN_CB)])
        for hh in range(hpb):
            s_scr[hh] = state[hh]

    ntb = t // SCAN_TB
    col = pl.BlockSpec((SCAN_TB, hpb * DH), lambda h, i: (_block_order(i, ntb, rev), h))
    return pl.pallas_call(
        body, name="scan_fwd_" + ("bw" if rev else "fw"), grid=(NH // hpb, ntb),
        in_specs=[col] * 4,
        out_specs=[col, pl.BlockSpec((hpb, SCAN_CB, DH, DH), lambda h, i: (h, _block_order(i, ntb, rev), 0, 0))],
        out_shape=[jax.ShapeDtypeStruct((t, D), F32), jax.ShapeDtypeStruct((NH, nc, DH, DH), F32)],
        scratch_shapes=[pltpu.VMEM((hpb, DH, DH), F32), pltpu.VMEM((hpb * SCAN_CB, CHUNK, DH), F32)],
        compiler_params=_cp("parallel", "arbitrary"),
    )(q, k, v, g)


def _scan_bwd(q, k, v, g, do, states, rev):
    t = q.shape[0]
    nc = t // CHUNK
    hpb = SCAN_HEADS_BWD

    def body(q_ref, k_ref, v_ref, g_ref, do_ref, st_ref, dq_ref, dk_ref, dv_ref, dg_ref, ds_scr, b_scr):
        consts = _scan_consts(rev)
        _, tri_t, lv, diag, diag_t = consts
        masks = [(pair, pair_t) for pair, pair_t, _ in lv] + [(diag, diag_t)]
        @pl.when(pl.program_id(1) == 0)
        def _():
            ds_scr[...] = jnp.zeros_like(ds_scr)

        tri = consts[0]
        dstate = {hh: ds_scr[hh] for hh in range(hpb)}

        def unit(hh, jj):
            sl = slice(hh * DH, (hh + 1) * DH)
            c = _chunk_in_block(SCAN_CB - 1 - jj, rev)
            rows = slice(c * CHUNK, (c + 1) * CHUNK)
            b_ref = b_scr.at[hh * SCAN_CB + jj]
            qc, kc, vc, gc = q_ref[rows, sl], k_ref[rows, sl], v_ref[rows, sl], g_ref[rows, sl]
            dob = do_ref[rows, sl].astype(BF16)
            vb = vc.astype(BF16)
            cum = _split_dot(tri, gc)
            b_ref[...] = cum
            da = _dot(dob, vb, NT)
            da_t = _dot(vb, dob, NT)
            yield
            terms = _chunk_terms(cum, b_ref, consts, rev)
            ops = [((qc * wq).astype(BF16), (kc * wk).astype(BF16)) for wq, wk in terms]
            tot = _colsum(gc)
            e_tot = jnp.exp(tot)
            e_b = jnp.exp(cum)
            e_t = jnp.exp(tot - cum)
            qeb = (qc * e_b).astype(BF16)
            keb = (kc * e_t).astype(BF16)
            dal = [(jnp.where(m, da, 0.0).astype(BF16), jnp.where(m_t, da_t, 0.0).astype(BF16)) for m, m_t in masks]
            yield
            ats = [_dot(ktb, qtb, NT) for qtb, ktb in ops]
            dqts = [_dot(d, ktb, NN) for (d, _), (_, ktb) in zip(dal, ops)]
            dkts = [_dot(d_t, qtb, NN) for (_, d_t), (qtb, _) in zip(dal, ops)]
            qd = _dot(dob, qeb, TN)
            yield
            a_t = jnp.zeros((CHUNK, CHUNK), F32)
            dq = jnp.zeros((CHUNK, DH), F32)
            dk = jnp.zeros((CHUNK, DH), F32)
            db = jnp.zeros((CHUNK, DH), F32)
            for at, dqt, dkt, (wq, wk), (qtb, ktb), (_, m_t) in zip(ats, dqts, dkts, terms, ops, masks):
                a_t = a_t + jnp.where(m_t, at, 0.0)
                dq = dq + dqt * wq
                dk = dk + dkt * wk
                db = db + dqt * qtb.astype(F32) - dkt * ktb.astype(F32)
            dv_intra = _dot(a_t.astype(BF16), dob, NN)
            st = st_ref[hh, c]
            stb = st.astype(BF16)
            dqe = _dot(dob, stb, NN)
            yield
            dst = dstate[hh]
            dstb = dst.astype(BF16)
            dstate[hh] = dst * e_tot + qd
            dv_ref[rows, sl] = (dv_intra + _dot(keb, dstb, NT)).astype(BF16)
            dke = _dot(vb, dstb, NN)
            yield
            qe = qeb.astype(F32)
            ke = keb.astype(F32)
            dq_ref[rows, sl] = (dq + dqe * e_b).astype(BF16)
            dk_ref[rows, sl] = (dk + dke * e_t).astype(BF16)
            db = db + dqe * qe - dke * ke
            dtot = _colsum(dstb.astype(F32) * stb.astype(F32)) * e_tot + _colsum(dke * ke)
            dg_ref[rows, sl] = _split_dot(tri_t, db) + dtot
            yield

        _run_staged([unit(hh, jj) for hh in range(hpb) for jj in range(SCAN_CB)])
        for hh in range(hpb):
            ds_scr[hh] = dstate[hh]

    ntb = t // SCAN_TB
    blk = lambda i: _block_order(ntb - 1 - i, ntb, rev)
    col = pl.BlockSpec((SCAN_TB, hpb * DH), lambda h, i: (blk(i), h))
    out = jax.ShapeDtypeStruct((t, D), F32)
    outb = jax.ShapeDtypeStruct((t, D), BF16)
    return pl.pallas_call(
        body, name="scan_bwd_" + ("bw" if rev else "fw"), grid=(NH // hpb, ntb),
        in_specs=[col] * 5 + [pl.BlockSpec((hpb, SCAN_CB, DH, DH), lambda h, i: (h, blk(i), 0, 0))],
        out_specs=[col] * 4,
        out_shape=[outb] * 3 + [out],
        scratch_shapes=[pltpu.VMEM((hpb, DH, DH), F32), pltpu.VMEM((hpb * SCAN_CB, CHUNK, DH), F32)],
        compiler_params=_cp("parallel", "arbitrary"),
    )(q, k, v, g, do, states)


W_IN_GRAD_CHUNKS = (("a", (0, 512)), ("b", (0, 128)), ("b", (128, 512)))
W_IN_REF = 6688


def _layout_w_in(w):
    return jnp.pad(w, ((0, 0), (0, W_IN_COLS - W_IN_REF)))


def _unlayout_w_in(d):
    return d[:, :W_IN_REF]


def _gate_cols(w):
    return jnp.pad(w, ((0, 0), (GOFF, GW - GOFF - D)))


def _gate_rows(w):
    return jnp.pad(w, ((GOFF, GW - GOFF - D), (0, 0)))


def _layout_wgk(w):
    r = w.shape[1]
    top = jnp.concatenate([w[0], jnp.zeros_like(w[0])], axis=1)
    bot = jnp.concatenate([jnp.zeros_like(w[1]), w[1]], axis=1)
    return jnp.concatenate([top, bot, jnp.zeros((DH - 2 * r, D), w.dtype)], axis=0)


def _unlayout_wgk(d, r=16):
    return jnp.stack([d[:r, :HW], d[r:2 * r, HW:]])


def _local_step(z, target, modc, modx, norms, onw, hg_lb, wgk, bgk, get_w_in, get_mix, get_ffn, send):
    n_pre1, n_post1, n_pre2, n_post2 = norms
    t = z.shape[0]
    tm = 768 if t % 768 == 0 else 256
    h1 = _prenorm(z, n_pre1, modc, modx, 0, 1, "prenorm1")
    w_in = get_w_in(h1)
    p = _matmul(h1, w_in, NN, BF16, "mm_in", tm, 512, D)
    q, v, k_f, k_b, g_f, g_b = _gates_fwd(p, hg_lb, wgk, bgk)
    o_f, st_f = _scan_fwd(q, k_f, v, g_f, False)
    o_b, st_b = _scan_fwd(q, k_b, v, g_b, True)
    y = _post_fwd(o_f, o_b, p, onw)
    w_br_hg, w_br_gla, w_out = get_mix(y)
    u1 = _matmul(y, w_br_hg, NN, BF16, "mm_br_hg", tm, GW, HW, a_off=0)
    u2 = _matmul(y, w_br_gla, NN, BF16, "mm_br_gla", tm, GW, HW, a_off=1)
    merged = _merge_fwd(p, u1, u2)
    y1 = _matmul(merged, w_out, NN, BF16, "mm_out", tm, 512, GW)
    z1, h2 = _mid_fwd(z, y1, n_post1, n_pre2, modc, modx)
    w_gu_t, w_down = get_ffn(h2)
    uv = _matmul(h2, w_gu_t, NT, BF16, "mm_gu", tm, D_FF // 2, D)
    act = _swiglu_fwd(uv)
    y2 = _matmul(act, w_down, NN, BF16, "mm_down", tm, 512, D_FF // 2)
    dz, dy2, loss_vec, sm_final = _final(z1, y2, target, n_post2, modc, modx)
    dact = _matmul(dy2, w_down, NT, BF16, "mm_down_dx", tm, D_FF // 2, D)
    d_w_down = _matmul(act, dy2, TN, BF16, "mm_down_dw", D_FF // 2, 512, t)
    duv = _swiglu_bwd(uv, dact)
    dh2 = _matmul(duv, w_gu_t, NN, BF16, "mm_gu_dx", tm, 512, D_FF // 2)
    d_w_gate_t = _matmul(duv, h2, TN, BF16, "mm_gate_dw", D_FF // 2, 512, t, a_off=0, m_out=D_FF)
    d_w_up_t = _matmul(duv, h2, TN, BF16, "mm_up_dw", D_FF // 2, 512, t, a_off=2, m_out=D_FF)
    dh2 = send(("w_down", "w_gate_t", "w_up_t"), (d_w_down, d_w_gate_t, d_w_up_t), dh2)
    dz, dy1, sm_mid = _mid_bwd(dh2, dz, z, z1, y1, n_post1, n_pre2, modc, modx)
    dmerged = _matmul(dy1, w_out, NT, BF16, "mm_out_dx", tm, GW, D)
    d_w_out = _matmul(merged, dy1, TN, BF16, "mm_out_dw", GW, 512, t)
    du1, du2, dgm = _merge_bwd(dmerged, p, u1, u2)
    dy_hg = _matmul(du1, w_br_hg, NT, BF16, "mm_br_hg_dx", tm, HW, GW)
    dy_gla = _matmul(du2, w_br_gla, NT, BF16, "mm_br_gla_dx", tm, HW, GW)
    d_w_br_hg = _matmul(y, du1, TN, BF16, "mm_br_hg_dw", HW, GW, t, a_off=0, m_out=HW)
    d_w_br_gla = _matmul(y, du2, TN, BF16, "mm_br_gla_dw", HW, GW, t, a_off=1, m_out=HW)
    dy_hg = send(("w_out", "w_br_hg", "w_br_gla"), (d_w_out, d_w_br_hg, d_w_br_gla), dy_hg)
    do, dgo, sm_post = _post_bwd(dy_hg, dy_gla, o_f, o_b, p, onw)
    dq_f, dk_f, dv_f, dg_f = _scan_bwd(q, k_f, v, g_f, do, st_f, False)
    dq_b, dk_b, dv_b, dg_b = _scan_bwd(q, k_b, v, g_b, do, st_b, True)
    dp, d_lb, d_wgk, d_bgk = _gates_bwd(p, hg_lb, wgk, bgk, dgm, dgo, dq_f, dq_b, dv_f, dv_b, dk_f, dk_b, dg_f, dg_b)
    d_w_in_a = _matmul(h1, dp, TN, BF16, "mm_in_dw_a", 512, 512, t, a_off=0, m_out=D // 2)
    dp = send(("w_in_a",), (d_w_in_a,), dp)
    d_w_in_b = _matmul(h1, dp, TN, BF16, "mm_in_dw_b", 512, 512, t, a_off=1, m_out=D // 2)
    dp = send(("w_in_b",), (d_w_in_b,), dp)
    dh1 = _matmul(dp, w_in, NT, BF16, "mm_in_dx", tm, 512, 1024)
    grad_x, sm_pre = _pre_bwd(dh1, dz, z, n_pre1, modc, modx)
    return dict(loss_vec=loss_vec, grad_x=grad_x, sm_final=sm_final, sm_mid=sm_mid, sm_post=sm_post, sm_pre=sm_pre,
                d_lb=d_lb, d_wgk=d_wgk, d_bgk=d_bgk)


MESH = pl.DeviceIdType.MESH
ANY = pl.BlockSpec(memory_space=pl.ANY)
N_REL = N_DEV - 1


def _place():
    return lax.axis_index("x"), lax.axis_index("y"), lax.axis_index("c")


def _slot(p):
    return 4 * p[0] + 2 * p[1] + p[2]


def _all_gather(arrays, name):
    n = len(arrays)

    def body(*refs):
        ins, outs = refs[:n], refs[n:2 * n]
        send_sems, recv_sems, local_sems = refs[2 * n:]
        x, y, c = _place()
        me, sibling = (x, y, c), (x, y, 1 - c)
        chips = [(1 - x, y), (x, 1 - y), (1 - x, 1 - y)]

        def copy(a, k, block, to, src=None):
            dst = outs[a].at[_slot(block)]
            return pltpu.make_async_remote_copy(
                src_ref=dst if src is None else src, dst_ref=dst,
                send_sem=send_sems.at[N_REL * a + k], recv_sem=recv_sems.at[N_REL * a + k],
                device_id=to, device_id_type=MESH)

        mine = [pltpu.make_async_copy(ins[a], outs[a].at[_slot(me)], local_sems.at[a]) for a in range(n)]
        for cp in mine:
            cp.start()
        first = []
        for a in range(n):
            first.append(copy(a, 0, me, sibling, src=ins[a]))
            first += [copy(a, 1 + j, me, (*chip, c), src=ins[a]) for j, chip in enumerate(chips)]
        for cp in first:
            cp.start()
        passed = []
        for j, chip in enumerate(chips):
            for a in range(n):
                copy(a, 1 + j, (*chip, c), me).wait_recv()
                fwd = copy(a, 4 + j, (*chip, c), sibling)
                fwd.start()
                passed.append(fwd)
        for a in range(n):
            copy(a, 0, sibling, me).wait_recv()
        for j, chip in enumerate(chips):
            for a in range(n):
                copy(a, 4 + j, (*chip, 1 - c), me).wait_recv()
        for cp in first + passed:
            cp.wait_send()
        for cp in mine:
            cp.wait()

    return pl.pallas_call(
        body, name=name,
        in_specs=[ANY] * n, out_specs=[ANY] * n,
        out_shape=[jax.ShapeDtypeStruct((N_DEV,) + a.shape, a.dtype) for a in arrays],
        scratch_shapes=[pltpu.SemaphoreType.DMA((N_REL * n,)), pltpu.SemaphoreType.DMA((N_REL * n,)),
                        pltpu.SemaphoreType.DMA((n,))],
    )(*arrays)


def _exchange(arrays, name):
    n = len(arrays)

    def body(*refs):
        ins, outs = refs[:n], refs[n:2 * n]
        send_sems, recv_sems, local_sems = refs[2 * n:]
        x, y, c = _place()
        me = _slot((x, y, c))
        mine = [pltpu.make_async_copy(ins[a].at[me], outs[a].at[me], local_sems.at[a]) for a in range(n)]
        for cp in mine:
            cp.start()
        copies = []
        for a in range(n):
            for k in range(1, N_DEV):
                flip = lambda v, bit: 1 - v if bit else v
                peer = (flip(x, k & 4), flip(y, k & 2), flip(c, k & 1))
                copies.append(pltpu.make_async_remote_copy(
                    src_ref=ins[a].at[_slot(peer)], dst_ref=outs[a].at[me],
                    send_sem=send_sems.at[N_REL * a + k - 1], recv_sem=recv_sems.at[N_REL * a + k - 1],
                    device_id=peer, device_id_type=MESH))
                copies[-1].start()
        i = 0
        for a in range(n):
            for k in range(1, N_DEV):
                flip = lambda v, bit: 1 - v if bit else v
                peer = (flip(x, k & 4), flip(y, k & 2), flip(c, k & 1))
                pltpu.make_async_remote_copy(
                    src_ref=ins[a].at[_slot(peer)], dst_ref=outs[a].at[_slot(peer)],
                    send_sem=send_sems.at[N_REL * a + k - 1], recv_sem=recv_sems.at[N_REL * a + k - 1],
                    device_id=peer, device_id_type=MESH).wait_recv()
                i += 1
        for cp in copies:
            cp.wait_send()
        for cp in mine:
            cp.wait()

    return pl.pallas_call(
        body, name=name,
        in_specs=[ANY] * n, out_specs=[ANY] * n,
        out_shape=[jax.ShapeDtypeStruct(a.shape, a.dtype) for a in arrays],
        scratch_shapes=[pltpu.SemaphoreType.DMA((N_REL * n,)), pltpu.SemaphoreType.DMA((N_REL * n,)),
                        pltpu.SemaphoreType.DMA((n,))],
    )(*arrays)


HBM = pl.BlockSpec(memory_space=pltpu.HBM)
SEM = pl.BlockSpec(memory_space=pltpu.SEMAPHORE)
EFFECT = pltpu.SideEffectType.DATAFLOW_SIDE_EFFECTING


def _peer_of(x, y, c, k):
    flip = lambda v, bit: 1 - v if bit else v
    return flip(x, k & 4), flip(y, k & 2), flip(c, k & 1)


def _view_whole(src, slot):
    return src


def _view_near(src, slot):
    return src


_view_near.peers = (1, 2, 4, 6)


def _view_block(src, slot):
    return src.at[slot]


W_IN_SHARD = W_IN_REF // N_DEV


def _view_window(rows):
    def view(src, slot):
        col0 = pl.multiple_of((W_IN_SHARD * slot // DH) * DH, DH)
        return src.at[pl.ds(rows[0], rows[1] - rows[0]), pl.ds(col0, D)]
    return view


def _split_copies(view, srcs, lands, send_sems, recv_sems, local_sems):
    x, y, c = _place()
    me = _slot((x, y, c))
    local, sends, waits = [], [], []
    for a, (src, land) in enumerate(zip(srcs, lands)):
        local.append(pltpu.make_async_copy(view(src, me), land.at[me], local_sems.at[a]))
        for k in getattr(view, "peers", range(1, N_DEV)):
            peer = _peer_of(x, y, c, k)
            mine = view(src, _slot(peer))
            sems = dict(send_sem=send_sems.at[N_REL * a + k - 1], recv_sem=recv_sems.at[N_REL * a + k - 1],
                        device_id=peer, device_id_type=MESH)
            sends.append(pltpu.make_async_remote_copy(src_ref=mine, dst_ref=land.at[me], **sems))
            waits.append(pltpu.make_async_remote_copy(src_ref=mine, dst_ref=land.at[_slot(peer)], **sems))
    return local, sends, waits


def _split_start(view, land_shapes, srcs, name, after):
    n = len(srcs)
    lands = [lax.empty(shp, s.dtype) for shp, s in zip(land_shapes, srcs)]

    def body(*refs):
        src_refs, land_refs = refs[:n], refs[n:2 * n]
        send_sems, recv_sems, local_sems = refs[2 * n + 1:2 * n + 4]
        token = refs[-1]
        local, sends, _ = _split_copies(view, src_refs, land_refs, send_sems, recv_sems, local_sems)
        for cp in local + sends:
            cp.start()
        token[...] = jnp.zeros_like(token)

    hbm = lambda a: pltpu.with_memory_space_constraint(a, pltpu.HBM)
    out = pl.pallas_call(
        body, name=name,
        out_shape=(pltpu.SemaphoreType.DMA((N_REL * n,)), pltpu.SemaphoreType.DMA((N_REL * n,)),
                   pltpu.SemaphoreType.DMA((n,)),
                   *[pltpu.HBM(s.shape, s.dtype) for s in srcs], *[pltpu.HBM(l.shape, l.dtype) for l in lands],
                   jax.ShapeDtypeStruct((8, DH), F32)),
        in_specs=[HBM] * (2 * n) + [ANY],
        out_specs=(SEM, SEM, SEM, *([HBM] * (2 * n)), pl.BlockSpec(memory_space=pltpu.VMEM)),
        input_output_aliases={i: 3 + i for i in range(2 * n)},
        compiler_params=pltpu.CompilerParams(has_side_effects=EFFECT),
    )(*[hbm(s) for s in srcs], *[hbm(l) for l in lands], after)
    handle = dict(view=view, n=n, sems=out[:3], srcs=list(out[3:3 + n]), lands=list(out[3 + n:3 + 2 * n]))
    return handle, out[-1]


def _split_wait(handle, name, after, srcs=None):
    view, n, sems, lands = handle["view"], handle["n"], handle["sems"], handle["lands"]
    srcs = handle["srcs"] if srcs is None else srcs
    afters = list(after) if isinstance(after, (list, tuple)) else [after]

    def body(*refs):
        src_refs, land_refs = refs[:n], refs[n:2 * n]
        send_sems, recv_sems, local_sems = refs[2 * n:2 * n + 3]
        local, _, waits = _split_copies(view, src_refs, land_refs, send_sems, recv_sems, local_sems)
        for cp in waits:
            cp.wait_send()
            cp.wait_recv()
        for cp in local:
            cp.wait()

    out = pl.pallas_call(
        body, name=name,
        out_shape=(*[pltpu.HBM(s.shape, s.dtype) for s in srcs], *[pltpu.HBM(l.shape, l.dtype) for l in lands]),
        in_specs=[HBM] * (2 * n) + [SEM, SEM, SEM] + [ANY] * len(afters),
        out_specs=tuple([HBM] * (2 * n)),
        input_output_aliases={i: i for i in range(2 * n)},
        compiler_params=pltpu.CompilerParams(has_side_effects=EFFECT),
    )(*srcs, *lands, *sems, *afters)
    handle["srcs"] = list(out[:n])
    return list(out[n:])


def _tie(x, token, name):
    def body(x_ref, t_ref, o_ref):
        pass

    return pl.pallas_call(
        body, name=name, out_shape=jax.ShapeDtypeStruct(x.shape, x.dtype),
        in_specs=[ANY, ANY], out_specs=ANY, input_output_aliases={0: 0},
    )(x, token)


def _forward_to_sibling(land, name):
    def body(land_ref, out_ref, send_sems, recv_sems):
        x, y, c = _place()
        sibling = (x, y, 1 - c)
        chips = [(1 - x, y), (x, 1 - y), (1 - x, 1 - y)]

        def copy(j, core):
            blk = _slot((*chips[j], core))
            return pltpu.make_async_remote_copy(src_ref=land_ref.at[blk], dst_ref=out_ref.at[blk],
                                                send_sem=send_sems.at[j], recv_sem=recv_sems.at[j],
                                                device_id=sibling, device_id_type=MESH)

        sends = [copy(j, c) for j in range(3)]
        for cp in sends:
            cp.start()
        for j in range(3):
            copy(j, 1 - c).wait_recv()
        for cp in sends:
            cp.wait_send()

    return pl.pallas_call(
        body, name=name, in_specs=[ANY], out_specs=ANY, input_output_aliases={0: 0},
        out_shape=jax.ShapeDtypeStruct(land.shape, land.dtype),
        scratch_shapes=[pltpu.SemaphoreType.DMA((3,)), pltpu.SemaphoreType.DMA((3,))],
    )(land)


def _mod_fwd(a, w, b):
    def body(a_ref, w_ref, b_ref, o_ref):
        o_ref[...] = _dot(_silu(a_ref[...]), w_ref[...], NN, precision=HI) + b_ref[...]

    return pl.pallas_call(
        body, name="mod_fwd", out_shape=jax.ShapeDtypeStruct((a.shape[0], w.shape[1]), F32),
        compiler_params=pltpu.CompilerParams(vmem_limit_bytes=VMEM_LIMIT),
    )(a, w, b)


def _mod_bwd(a, d, w):
    def body(a_ref, d_ref, w_ref, dw_ref, dc_ref):
        av = a_ref[...]
        dv = d_ref[...]
        dw_ref[...] = _dot(_silu(av), dv, TN, precision=HI)
        da = _dot(dv[0:8, :], w_ref[...], NT, precision=HI) * _dsilu(av[0:8, :])
        row = lax.broadcasted_iota(jnp.int32, da.shape, 0)
        dc_ref[...] = jnp.where(row == 0, da, 0.0)

    return pl.pallas_call(
        body, name="mod_bwd",
        out_shape=[jax.ShapeDtypeStruct(w.shape, F32), jax.ShapeDtypeStruct((8, w.shape[0]), F32)],
        compiler_params=pltpu.CompilerParams(vmem_limit_bytes=VMEM_LIMIT),
    )(a, d, w)


def _sum_devices(g):
    def body(g_ref, o_ref):
        acc = g_ref[0]
        for i in range(1, g.shape[0]):
            acc = acc + g_ref[i]
        o_ref[...] = acc

    return pl.pallas_call(body, name="sum_devices_%d" % g.shape[1],
                          out_shape=jax.ShapeDtypeStruct(g.shape[1:], F32))(g)


def _sum_windows(g, name):
    n, r, c = g.shape
    tr = 128

    def body(g_ref, o_ref):
        x, y, cc = _place()
        lane0 = (W_IN_SHARD * _slot((x, y, cc))) % DH
        acc = g_ref[0].astype(F32)
        for i in range(1, n):
            acc = acc + g_ref[i].astype(F32)
        o_ref[...] = pltpu.roll(acc, (c - lane0) % c, 1).T

    return pl.pallas_call(
        body, name=name, grid=(r // tr,),
        in_specs=[pl.BlockSpec((n, tr, c), lambda i: (0, i, 0))],
        out_specs=pl.BlockSpec((c, tr), lambda i: (0, i)),
        out_shape=jax.ShapeDtypeStruct((c, r), F32),
        compiler_params=_cp("parallel"),
    )(g)


def _adam_rows(r, c, n):
    budget = 6 * 1024 * 1024
    best = None
    for tr in range(16, r + 1, 16):
        if r % tr == 0 and tr * c * (2 * n + 28) <= budget:
            best = tr
    return best if best is not None else r


def _adamw(g, w, m, v, name):
    n, r, c = g.shape
    tr = _adam_rows(r, c, n)
    bc1 = 1.0 - ADAM_B1 ** ADAM_STEP
    bc2 = 1.0 - ADAM_B2 ** ADAM_STEP

    def body(g_ref, w_ref, m_ref, v_ref, go_ref, d_ref, mo_ref, vo_ref):
        grad = g_ref[0].astype(F32)
        for i in range(1, n):
            grad = grad + g_ref[i].astype(F32)
        go_ref[...] = grad
        m_new = ADAM_B1 * m_ref[...] + (1.0 - ADAM_B1) * grad
        v_new = ADAM_B2 * v_ref[...] + (1.0 - ADAM_B2) * (grad * grad)
        mo_ref[...] = m_new
        vo_ref[...] = v_new
        d_ref[...] = -ADAM_LR * ((m_new / bc1) / (jnp.sqrt(v_new / bc2) + ADAM_EPS) + ADAM_WD * w_ref[...])

    blk = pl.BlockSpec((tr, c), lambda i: (i, 0))
    out = jax.ShapeDtypeStruct((r, c), F32)
    return pl.pallas_call(
        body, name=name, grid=(r // tr,),
        in_specs=[pl.BlockSpec((n, tr, c), lambda i: (0, i, 0)), blk, blk, blk],
        out_specs=[blk] * 4, out_shape=[out] * 4,
        compiler_params=_cp("parallel"),
    )(g, w, m, v)


def kernel(x, c, ctx, c_ctx, w_mod, b_mod, norm_pre1, norm_post1, norm_pre2, norm_post2, w_in, hg_lb, hg_onorm, gla_w_gk, gla_b_gk, gla_onorm, w_br_hg, w_br_gla, w_out, w_ff_gate, w_ff_up, w_ff_down, loss_target, m_c_ctx, m_w_mod, m_b_mod, m_norm_pre1, m_norm_post1, m_norm_pre2, m_norm_post2, m_w_in, m_hg_lb, m_hg_onorm, m_gla_w_gk, m_gla_b_gk, m_gla_onorm, m_w_br_hg, m_w_br_gla, m_w_out, m_w_ff_gate, m_w_ff_up, m_w_ff_down, v_c_ctx, v_w_mod, v_b_mod, v_norm_pre1, v_norm_post1, v_norm_pre2, v_norm_post2, v_w_in, v_hg_lb, v_hg_onorm, v_gla_w_gk, v_gla_b_gk, v_gla_onorm, v_w_br_hg, v_w_br_gla, v_w_out, v_w_ff_gate, v_w_ff_up, v_w_ff_down):
    xi, yi, ci = lax.axis_index("x"), lax.axis_index("y"), lax.axis_index("c")
    me = 4 * xi + 2 * yi + ci
    t = CTX + x.shape[1]

    c_all, lb_g, wgk_g, bgk_g = _all_gather([c, hg_lb, gla_w_gk[0], gla_b_gk[0]], "ag_small")
    tr_ = lambda a: jnp.swapaxes(a[0], 0, 1)
    big = [w_in[0], w_br_hg[0], w_br_gla[0], w_out[0], tr_(w_ff_gate), tr_(w_ff_up), w_ff_down[0]]
    big_bf = [w.astype(BF16) for w in big]
    cols = lambda g: jnp.transpose(g, (1, 0, 2)).reshape(g.shape[1], N_DEV * g.shape[2])

    def get_w_in(after):
        land, = _split_wait(w_in_handle, "ag_w_in_wait", after)
        return _layout_w_in(cols(_forward_to_sibling(land, "ag_w_in_forward")))

    def get_mix(after):
        g_brh, g_brg, g_out = _split_wait(mix_handle, "ag_mix_wait", after)
        return _gate_cols(cols(g_brh)), _gate_cols(cols(g_brg)), _gate_rows(g_out.reshape(D, D))

    def get_ffn(after):
        g_gate, g_up, g_down = _split_wait(ffn_handle, "ag_ffn_wait", after)
        return (g_gate.reshape(D_FF, D), g_up.reshape(D_FF, D)), g_down.reshape(D_FF, D)

    hg_lb_full = jnp.transpose(lb_g, (1, 2, 0, 3)).reshape(2, 2, HW)
    wgk_k = _layout_wgk(jnp.transpose(wgk_g, (1, 2, 0, 3)).reshape(2, 16, HW)).astype(BF16)
    bgk_k = jnp.transpose(bgk_g, (1, 0, 2)).reshape(1, D)
    onw = jnp.concatenate([jnp.tile(hg_onorm, (1, NH // 2)), jnp.tile(gla_onorm, (1, NH // 2))], axis=1)

    n_mod = w_mod.shape[2]
    a9 = jnp.concatenate([c_ctx[None], c_all[:, 0], jnp.zeros((16 - 1 - N_DEV, D), F32)], axis=0)
    b_loc = lax.dynamic_slice(b_mod, (0, me * n_mod), (1, n_mod))
    s_loc = _mod_fwd(a9, w_mod[0], b_loc)
    s_all, = _all_gather([s_loc], "ag_mod")
    mod_all = jnp.transpose(s_all, (1, 0, 2)).reshape(16, N_DEV * n_mod)
    pad8 = lambda m: jnp.concatenate([m.reshape(6, D), jnp.zeros((2, D), F32)], axis=0)
    modc = pad8(mod_all[0])
    modx = pad8(lax.dynamic_slice(mod_all, (1 + me, 0), (1, N_DEV * n_mod))[0])

    gathered = lambda arrs: [(N_DEV,) + a.shape for a in arrs]
    w_in_handle, tok = _split_start(_view_near, gathered(big_bf[:1]), big_bf[:1], "ag_w_in_start", s_all)
    mix_handle, tok = _split_start(_view_whole, gathered(big_bf[1:4]), big_bf[1:4], "ag_mix_start", tok)
    ffn_handle, tok = _split_start(_view_whole, gathered(big_bf[4:]), big_bf[4:], "ag_ffn_start", tok)

    z = _tie(jnp.concatenate([ctx[0], x[0]], axis=0), tok, "tie_z")
    norms = (norm_pre1, norm_post1, norm_pre2, norm_post2)
    shard = lambda d: jnp.transpose(d.reshape(d.shape[0], N_DEV, -1), (1, 0, 2)).astype(BF16)
    rowshard = lambda d: d.reshape(N_DEV, d.shape[0] // N_DEV, d.shape[1]).astype(BF16)
    sent, w_in_grad = [], {}

    def send_w_in(i, x_after):
        half, rows = W_IN_GRAD_CHUNKS[i]
        handle, tok = _split_start(_view_window(rows), [(N_DEV, rows[1] - rows[0], D)], w_in_grad[half],
                                   "grads_w_in%d_start" % i, x_after)
        w_in_grad[half] = handle["srcs"]
        sent.append(("w_in%d" % i, ["w_in#%d" % i], handle))
        return _tie(x_after, tok, "tie_w_in%d" % i)

    def send(names, grads, x_after):
        if names == ("w_in_a",):
            w_in_grad["a"] = list(grads)
            return send_w_in(0, x_after)
        if names == ("w_in_b",):
            w_in_grad["b"] = list(grads)
            return x_after
        arrs, leaves = [], []
        for nm, g in zip(names, grads):
            if nm in ("w_gate_t", "w_up_t"):
                arrs.append(rowshard(g))
                leaves.append({"w_gate_t": "w_ff_gate", "w_up_t": "w_ff_up"}[nm])
            elif nm == "w_down":
                arrs.append(rowshard(g))
                leaves.append("w_ff_down")
            elif nm == "w_out":
                arrs.append(rowshard(g[GOFF:GOFF + D]))
                leaves.append(nm)
            else:
                arrs.append(shard(g[:, GOFF:GOFF + D]))
                leaves.append(nm)
        handle, tok = _split_start(_view_block, [a.shape for a in arrs], arrs, "grads_%s_start" % names[0], x_after)
        sent.append((names[0], leaves, handle))
        return _tie(x_after, tok, "tie_" + names[0])

    r = _local_step(z, loss_target[0], modc, modx, norms, onw, hg_lb_full, wgk_k, bgk_k,
                    get_w_in, get_mix, get_ffn, send)
    grad_x = r["grad_x"][None]

    sm_pre, sm_mid, sm_fin = r["sm_pre"], r["sm_mid"], r["sm_final"]
    dmodc = jnp.stack([sm_pre[0], sm_pre[2], sm_mid[4], sm_mid[0], sm_mid[2], sm_fin[0]]).reshape(-1)
    dmodx = jnp.stack([sm_pre[1], sm_pre[3], sm_mid[5], sm_mid[1], sm_mid[3], sm_fin[1]]).reshape(-1)
    on = r["sm_post"][0].reshape(NH, DH)
    pieces = [dmodc, dmodx, sm_pre[4], sm_mid[7], sm_mid[6], sm_fin[2], on[:NH // 2].sum(0), on[NH // 2:].sum(0),
              r["d_lb"][:2].reshape(-1), _unlayout_wgk(r["d_wgk"]).reshape(-1), r["d_bgk"][0]]
    loss_local = (0.5 / D) * jnp.sum(r["loss_vec"])
    pieces.append(jnp.concatenate([loss_local.reshape(1), jnp.zeros((DH - 1,), F32)]))
    sizes = [p.shape[0] for p in pieces]
    pack = jnp.concatenate(pieces).reshape(-1, DH)
    pack_all, = _all_gather([pack], "ag_small_grads")
    pack_all = send_w_in(1, pack_all)
    tot = _sum_devices(pack_all).reshape(-1)
    offs = [sum(sizes[:i]) for i in range(len(sizes))]
    part = lambda i: tot[offs[i]:offs[i] + sizes[i]]
    dmodc_t, dmodx_t = part(0), part(1)
    g_b_mod = (dmodc_t + dmodx_t)[None]
    g_norms = [part(i)[None] for i in (2, 3, 4, 5)]
    g_hg_on, g_gla_on = part(6)[None], part(7)[None]
    lb0 = lax.dynamic_slice(part(8).reshape(2, HW), (0, me * (HW // N_DEV)), (2, HW // N_DEV))
    g_hg_lb = jnp.stack([lb0, -lb0])
    g_wgk = lax.dynamic_slice(part(9).reshape(2, 16, HW), (0, 0, me * (HW // N_DEV)), (2, 16, HW // N_DEV))[None]
    g_bgk = lax.dynamic_slice(part(10).reshape(2, HW), (0, me * (HW // N_DEV)), (2, HW // N_DEV))[None]
    loss = part(11)[0]

    dmx_all = pack_all.reshape(N_DEV, -1)[:, sizes[0]:sizes[0] + sizes[1]]
    d9 = jnp.concatenate([lax.dynamic_slice(dmodc_t[None], (0, me * n_mod), (1, n_mod)),
                          lax.dynamic_slice(dmx_all, (0, me * n_mod), (N_DEV, n_mod)),
                          jnp.zeros((16 - 1 - N_DEV, n_mod), F32)], axis=0)
    g_w_mod, dcc_part = _mod_bwd(a9, d9, w_mod[0])
    dcc_all, = _all_gather([dcc_part], "ag_c_ctx")
    dcc_all = send_w_in(2, dcc_all)
    g_c_ctx = _sum_devices(dcc_all)[0]

    recv = {}
    for first, leaves, handle in sent:
        if not first.startswith("w_in"):
            recv.update(zip(leaves, _split_wait(handle, "grads_%s_wait" % first, g_c_ctx)))
    moms = [(m_w_in, v_w_in), (m_w_br_hg, v_w_br_hg), (m_w_br_gla, v_w_br_gla), (m_w_out, v_w_out),
            (m_w_ff_gate, v_w_ff_gate), (m_w_ff_up, v_w_ff_up), (m_w_ff_down, v_w_ff_down)]
    names = ["w_in", "w_br_hg", "w_br_gla", "w_out", "w_ff_gate", "w_ff_up", "w_ff_down"]
    res = {}

    def update(nm, w, m, v):
        if nm in ("w_ff_gate", "w_ff_up"):
            outs = _adamw(recv[nm], w, tr_(m), tr_(v), "adamw_" + nm)
            res[nm] = [jnp.swapaxes(o, 0, 1)[None] for o in outs]
        else:
            res[nm] = [o[None] for o in _adamw(recv[nm], w, m[0], v[0], "adamw_" + nm)]

    for nm, w, (m, v) in list(zip(names, big, moms))[1:]:
        update(nm, w, m, v)
    res["w_mod"] = [o[None] for o in _adamw(g_w_mod[None], w_mod[0], m_w_mod[0], v_w_mod[0], "adamw_w_mod")]

    small = [("c_ctx", c_ctx, m_c_ctx, v_c_ctx, g_c_ctx), ("b_mod", b_mod, m_b_mod, v_b_mod, g_b_mod),
             ("norm_pre1", norm_pre1, m_norm_pre1, v_norm_pre1, g_norms[0]),
             ("norm_post1", norm_post1, m_norm_post1, v_norm_post1, g_norms[1]),
             ("norm_pre2", norm_pre2, m_norm_pre2, v_norm_pre2, g_norms[2]),
             ("norm_post2", norm_post2, m_norm_post2, v_norm_post2, g_norms[3]),
             ("hg_lb", hg_lb, m_hg_lb, v_hg_lb, g_hg_lb), ("hg_onorm", hg_onorm, m_hg_onorm, v_hg_onorm, g_hg_on),
             ("gla_w_gk", gla_w_gk, m_gla_w_gk, v_gla_w_gk, g_wgk), ("gla_b_gk", gla_b_gk, m_gla_b_gk, v_gla_b_gk, g_bgk),
             ("gla_onorm", gla_onorm, m_gla_onorm, v_gla_onorm, g_gla_on)]
    flat = lambda k: jnp.concatenate([s[k].reshape(-1) for s in small]).reshape(-1, DH)
    outs = _adamw(flat(4)[None], flat(1), flat(2), flat(3), "adamw_small")
    off = 0
    for nm, w, _, _, _ in small:
        res[nm] = [o.reshape(-1)[off:off + w.size].reshape(w.shape) for o in outs]
        off += w.size

    done = [res[nm][0] for nm in names[1:]] + [res["w_mod"][0], outs[0]]
    sums = []
    for i, (first, leaves, handle) in enumerate(s for s in sent if s[0].startswith("w_in")):
        half = W_IN_GRAD_CHUNKS[i][0]
        land, = _split_wait(handle, "grads_%s_wait" % first, done, srcs=w_in_grad[half])
        w_in_grad[half] = handle["srcs"]
        sums.append(_sum_windows(land, "sum_windows%d" % i))
    g_t = jnp.concatenate(sums, axis=1)[:W_IN_SHARD]
    lin = lambda a: a.reshape(W_IN_SHARD * D // DH, DH)
    outs = _adamw(lin(g_t)[None], lin(tr_(w_in)), lin(tr_(m_w_in)), lin(tr_(v_w_in)), "adamw_w_in")
    res["w_in"] = [jnp.swapaxes(o.reshape(W_IN_SHARD, D), 0, 1)[None] for o in outs]

    order = ["c_ctx", "w_mod", "b_mod", "norm_pre1", "norm_post1", "norm_pre2", "norm_post2", "w_in", "hg_lb",
             "hg_onorm", "gla_w_gk", "gla_b_gk", "gla_onorm", "w_br_hg", "w_br_gla", "w_out", "w_ff_gate", "w_ff_up",
             "w_ff_down"]
    return (loss, grad_x, *[res[n][k] for k in range(4) for n in order])
```

```python
import functools

import jax
import jax.numpy as jnp
from jax import lax
from jax.experimental import pallas as pl
from jax.experimental.pallas import tpu as pltpu

F32 = jnp.float32
BF16 = jnp.bfloat16
HI = lax.Precision.HIGHEST

N_DEV = 8
D = 1024
CTX = 256
HW = 512
DH = 128
NH = 8
D_FF = 2816
EPS = 1e-6
GLA_NORM = 16.0
CHUNK = 64
TR = 256
NCT = CTX // TR
W_IN_COLS = 7168
MAIN0 = 0
LR0 = 4608
GW = 1152
GOFF = 32
GATE_HG0 = LR0
GATE_GLA0 = LR0 + D
LEVELS = (32, 16, 8)
EXP_CLAMP = 80.0
VMEM_LIMIT = 48 * 1024 * 1024

ADAM_LR, ADAM_B1, ADAM_B2, ADAM_EPS, ADAM_WD, ADAM_STEP = 0.001, 0.9, 0.999, 1e-08, 0.01, 10


def _cp(*sem):
    return pltpu.CompilerParams(dimension_semantics=sem, vmem_limit_bytes=VMEM_LIMIT)


def _sig(x):
    return jax.nn.sigmoid(x)


def _silu(x):
    return x * _sig(x)


def _dsilu(x):
    s = _sig(x)
    return s * (1.0 + x * (1.0 - s))


def _rstd(x):
    return lax.rsqrt(jnp.mean(x * x, axis=-1, keepdims=True) + EPS)


def _rms_bwd(a, y, r):
    return r * (a - y * (r * r) * jnp.mean(a * y, axis=-1, keepdims=True))


def _colsum(x):
    return jnp.sum(x, axis=0, keepdims=True)


def _dot(a, b, dims, precision=None):
    return lax.dot_general(a, b, (dims, ((), ())), preferred_element_type=F32, precision=precision)


NN = ((1,), (0,))
NT = ((1,), (1,))
TN = ((0,), (0,))

SCAN_HEADS_FWD = 4
SCAN_HEADS_BWD = 4


def _split_dot(m, x):
    mb = m.astype(BF16)
    x1 = x.astype(BF16)
    r1 = x - x1.astype(F32)
    x2 = r1.astype(BF16)
    x3 = (r1 - x2.astype(F32)).astype(BF16)
    return _dot(mb, x1, NN) + _dot(mb, x2, NN) + _dot(mb, x3, NN)


def _matmul(a, b, dims, out_dtype, name, tm, tn, tk, a_off=0, m_out=None):
    pair = isinstance(b, (tuple, list))
    bs = list(b) if pair else [b]
    b1 = bs[0]
    rows = b1.shape[0] * len(bs)
    half = None
    if dims == NN:
        m, k, n = a.shape[0], rows, b1.shape[1]
        a_spec = pl.BlockSpec((tm, tk), lambda i, j, kk: (i, kk + a_off))
        half = b1.shape[0] // tk
        b_maps = [lambda i, j, kk: (kk, j)] if not pair else [
            lambda i, j, kk: (jnp.minimum(kk, half - 1), j), lambda i, j, kk: (jnp.maximum(kk - half, 0), j)]
        b_specs = [pl.BlockSpec((tk, tn), f) for f in b_maps]
        axis = 2
    elif dims == NT:
        m, k, n = a.shape[0], b1.shape[1], rows
        a_spec = pl.BlockSpec((tm, tk), lambda i, j, kk: (i, kk + a_off))
        half = b1.shape[0] // tn
        b_maps = [lambda i, j, kk: (j, kk)] if not pair else [
            lambda i, j, kk: (jnp.minimum(j, half - 1), kk), lambda i, j, kk: (jnp.maximum(j - half, 0), kk)]
        b_specs = [pl.BlockSpec((tn, tk), f) for f in b_maps]
        axis = 1
    else:
        assert not pair
        m, k = (a.shape[1] if m_out is None else m_out), a.shape[0]
        n = b1.shape[1]
        a_spec = pl.BlockSpec((tk, tm), lambda i, j, kk: (kk, i + a_off))
        b_specs = [pl.BlockSpec((tk, tn), lambda i, j, kk: (kk, j))]
    assert m % tm == 0 and n % tn == 0 and k % tk == 0, (name, m, n, k, tm, tn, tk)
    nk = k // tk
    nb = len(bs)

    def body(a_ref, *refs):
        o_ref = refs[nb]
        if pair:
            bv = jnp.where(pl.program_id(axis) < half, refs[0][...], refs[1][...])
        else:
            bv = refs[0][...]
        part = _dot(a_ref[...], bv, dims)
        if nk == 1:
            o_ref[...] = part.astype(o_ref.dtype)
            return
        acc_ref = refs[nb + 1]
        kk = pl.program_id(2)

        @pl.when(kk == 0)
        def _():
            acc_ref[...] = part

        @pl.when(kk > 0)
        def _():
            acc_ref[...] += part

        @pl.when(kk == nk - 1)
        def _():
            o_ref[...] = acc_ref[...].astype(o_ref.dtype)

    return pl.pallas_call(
        body,
        name=name,
        grid=(m // tm, n // tn, nk),
        in_specs=[a_spec] + b_specs,
        out_specs=pl.BlockSpec((tm, tn), lambda i, j, kk: (i, j)),
        out_shape=jax.ShapeDtypeStruct((m, n), out_dtype),
        scratch_shapes=[] if nk == 1 else [pltpu.VMEM((tm, tn), F32)],
        compiler_params=_cp("parallel", "parallel", "arbitrary"),
    )(a, *bs)


def _row(c):
    return pl.BlockSpec((TR, c), lambda i: (i, 0))


def _rowcol(width, cb):
    return pl.BlockSpec((TR, width), lambda i: (i, cb))


def _full(shape):
    return pl.BlockSpec(shape, lambda i: (0,) * len(shape))


def _mod_row(mc_ref, mx_ref, k, is_ctx):
    return jnp.where(is_ctx, mc_ref[k:k + 1, :], mx_ref[k:k + 1, :])


def _acc_row(ref, k, val):
    ref[k:k + 1, :] += val


def _acc_mod(ref, k, is_ctx, val):
    zero = jnp.zeros_like(val)
    ref[k:k + 1, :] += jnp.where(is_ctx, val, zero)
    ref[k + 1:k + 2, :] += jnp.where(is_ctx, zero, val)


def _prenorm(z, nw, modc, modx, i_shift, i_scale, name):
    t = z.shape[0]

    def body(z_ref, nw_ref, mc_ref, mx_ref, h_ref):
        is_ctx = pl.program_id(0) < NCT
        x = z_ref[...]
        n = x * _rstd(x) * nw_ref[...]
        h = n * (1.0 + _mod_row(mc_ref, mx_ref, i_scale, is_ctx)) + _mod_row(mc_ref, mx_ref, i_shift, is_ctx)
        h_ref[...] = h.astype(BF16)

    return pl.pallas_call(
        body, name=name, grid=(t // TR,),
        in_specs=[_row(D), _full((1, D)), _full((8, D)), _full((8, D))],
        out_specs=_row(D),
        out_shape=jax.ShapeDtypeStruct((t, D), BF16),
        compiler_params=_cp("parallel"),
    )(z, nw, modc, modx)


def _hg_lb(lb_ref, d):
    a0 = lb_ref[0, d:d + 1, :]
    a1 = lb_ref[1, d:d + 1, :]
    mx = jnp.maximum(a0, a1)
    e0 = jnp.exp(a0 - mx)
    e1 = jnp.exp(a1 - mx)
    return e0 / (e0 + e1)


def _log_sigmoid(x):
    return jnp.minimum(x, 0.0) - jnp.log(1.0 + jnp.exp(-jnp.abs(x)))


def _gates_fwd(p, hg_lb, wgk, bgk):
    t = p.shape[0]
    seg = lambda j: _rowcol(HW, MAIN0 // HW + j)

    def body(hq_ref, hi_ref, hf_ref, hb_ref, gq_ref, gk_ref, gv_ref, lr_ref, lb_ref, wgk_ref, bgk_ref,
             q_ref, v_ref, kf_ref, kb_ref, gf_ref, gb_ref):
        q_ref[:, :HW] = _silu(hq_ref[...].astype(F32)).astype(BF16)
        q_ref[:, HW:] = (gq_ref[...].astype(F32) * (DH ** -0.5)).astype(BF16)
        v_ref[:, :HW] = hi_ref[...]
        v_ref[:, HW:] = gv_ref[...]
        xg = _dot(lr_ref[...].astype(BF16), wgk_ref[...], NN) + bgk_ref[...]
        for d, (raw_ref, k_ref, g_ref) in enumerate(((hf_ref, kf_ref, gf_ref), (hb_ref, kb_ref, gb_ref))):
            lbd = _hg_lb(lb_ref, d)
            f = lbd + (1.0 - lbd) * _sig(raw_ref[...].astype(F32))
            k_ref[:, :HW] = (1.0 - f).astype(BF16)
            k_ref[:, HW:] = gk_ref[...]
            g_ref[:, :HW] = jnp.log(f)
            g_ref[:, HW:] = _log_sigmoid(xg[:, d * HW:(d + 1) * HW]) * (1.0 / GLA_NORM)

    out = jax.ShapeDtypeStruct((t, D), F32)
    outb = jax.ShapeDtypeStruct((t, D), BF16)
    return pl.pallas_call(
        body, name="gates_fwd", grid=(t // TR,),
        in_specs=[seg(0), seg(1), seg(2), seg(3), seg(5), seg(6), seg(7), _rowcol(DH, LR0 // DH),
                  _full((2, 2, HW)), _full((DH, D)), _full((1, D))],
        out_specs=[_row(D)] * 6,
        out_shape=[outb] * 4 + [out] * 2,
        compiler_params=_cp("parallel"),
    )(p, p, p, p, p, p, p, p, hg_lb, wgk, bgk)


def _post_fwd(o_fw, o_bw, p, onw):
    t = o_fw.shape[0]

    def body(of_ref, ob_ref, g1_ref, g2_ref, w_ref, y_ref):
        for h in range(NH):
            sl = slice(h * DH, (h + 1) * DH)
            o = of_ref[:, sl] + ob_ref[:, sl]
            g_ref = g1_ref if h < NH // 2 else g2_ref
            gs = slice((h % (NH // 2)) * DH, (h % (NH // 2) + 1) * DH)
            n = o * _rstd(o) * w_ref[:, sl]
            y_ref[:, sl] = (n * _silu(g_ref[:, gs].astype(F32))).astype(BF16)

    return pl.pallas_call(
        body, name="post_fwd", grid=(t // TR,),
        in_specs=[_row(D), _row(D), _rowcol(HW, MAIN0 // HW + 4), _rowcol(HW, MAIN0 // HW + 8), _full((1, D))],
        out_specs=_row(D),
        out_shape=jax.ShapeDtypeStruct((t, D), BF16),
        compiler_params=_cp("parallel"),
    )(o_fw, o_bw, p, p, onw)


def _gate_window_specs(col0):
    return [_rowcol(HW, col0 // HW), _rowcol(HW, col0 // HW + 1), _rowcol(DH, (col0 + 2 * HW) // DH)]


def _gate_window(refs):
    return jnp.concatenate([r[...].astype(F32) for r in refs], axis=1)


def _merge_fwd(p, u1, u2):
    t = p.shape[0]

    def body(a0, a1, a2, b0, b1, b2, u1_ref, u2_ref, m_ref):
        f = lambda r: r[...].astype(F32)
        m_ref[...] = (_sig(_gate_window((a0, a1, a2))) * f(u1_ref)
                      + _sig(_gate_window((b0, b1, b2))) * f(u2_ref)).astype(BF16)

    return pl.pallas_call(
        body, name="merge_fwd", grid=(t // TR,),
        in_specs=_gate_window_specs(GATE_HG0) + _gate_window_specs(GATE_GLA0) + [_row(GW), _row(GW)],
        out_specs=_row(GW),
        out_shape=jax.ShapeDtypeStruct((t, GW), BF16),
        compiler_params=_cp("parallel"),
    )(p, p, p, p, p, p, u1, u2)


def _mid_fwd(z, y1, nw_post, nw_pre, modc, modx):
    t = z.shape[0]

    def body(z_ref, y_ref, wpo_ref, wpr_ref, mc_ref, mx_ref, z1_ref, h_ref):
        is_ctx = pl.program_id(0) < NCT
        y = y_ref[...].astype(F32)
        z1 = z_ref[...] + _mod_row(mc_ref, mx_ref, 2, is_ctx) * (y * _rstd(y) * wpo_ref[...])
        z1_ref[...] = z1
        n = z1 * _rstd(z1) * wpr_ref[...]
        h = n * (1.0 + _mod_row(mc_ref, mx_ref, 4, is_ctx)) + _mod_row(mc_ref, mx_ref, 3, is_ctx)
        h_ref[...] = h.astype(BF16)

    return pl.pallas_call(
        body, name="mid_fwd", grid=(t // TR,),
        in_specs=[_row(D), _row(D), _full((1, D)), _full((1, D)), _full((8, D)), _full((8, D))],
        out_specs=[_row(D), _row(D)],
        out_shape=[jax.ShapeDtypeStruct((t, D), F32), jax.ShapeDtypeStruct((t, D), BF16)],
        compiler_params=_cp("parallel"),
    )(z, y1, nw_post, nw_pre, modc, modx)


def _swiglu_fwd(uv):
    t = uv.shape[0]

    def body(u_ref, v_ref, a_ref):
        a_ref[...] = (_silu(u_ref[...].astype(F32)) * v_ref[...].astype(F32)).astype(BF16)

    return pl.pallas_call(
        body, name="swiglu_fwd", grid=(t // TR,),
        in_specs=[_rowcol(D_FF, 0), _rowcol(D_FF, 1)],
        out_specs=_row(D_FF),
        out_shape=jax.ShapeDtypeStruct((t, D_FF), BF16),
        compiler_params=_cp("parallel"),
    )(uv, uv)


def _swiglu_bwd(uv, da):
    t = uv.shape[0]

    def body(u_ref, v_ref, da_ref, d_ref):
        u = u_ref[...].astype(F32)
        d = da_ref[...].astype(F32)
        d_ref[:, :D_FF] = (d * v_ref[...].astype(F32) * _dsilu(u)).astype(BF16)
        d_ref[:, D_FF:] = (d * _silu(u)).astype(BF16)

    return pl.pallas_call(
        body, name="swiglu_bwd", grid=(t // TR,),
        in_specs=[_rowcol(D_FF, 0), _rowcol(D_FF, 1), _row(D_FF)],
        out_specs=_row(2 * D_FF),
        out_shape=jax.ShapeDtypeStruct((t, 2 * D_FF), BF16),
        compiler_params=_cp("parallel"),
    )(uv, uv, da)


def _final(z1, y2, target, nw, modc, modx):
    t = z1.shape[0]

    def body(z1_ref, y_ref, tg_ref, w_ref, mc_ref, mx_ref, dz_ref, dy_ref, loss_ref, sm_ref):
        i = pl.program_id(0)
        is_ctx = i < NCT

        @pl.when(i == 0)
        def _():
            loss_ref[...] = jnp.zeros_like(loss_ref)
            sm_ref[...] = jnp.zeros_like(sm_ref)

        g = _mod_row(mc_ref, mx_ref, 5, is_ctx)
        y = y_ref[...].astype(F32)
        r = _rstd(y)
        w = w_ref[...]
        yr = y * r
        n = yr * w
        e = z1_ref[...] + g * n - tg_ref[...]
        lat = jnp.where(is_ctx, 0.0, 1.0)
        loss_ref[...] += lat * _colsum(e * e)
        dz = e * (lat / D)
        dz_ref[...] = dz
        _acc_mod(sm_ref, 0, is_ctx, _colsum(dz * n))
        dn = dz * g
        _acc_row(sm_ref, 2, _colsum(dn * yr))
        dy_ref[...] = _rms_bwd(dn * w, y, r).astype(BF16)

    return pl.pallas_call(
        body, name="final", grid=(t // TR,),
        in_specs=[_row(D), _row(D), pl.BlockSpec((TR, D), lambda i: (jnp.maximum(i - NCT, 0), 0)),
                  _full((1, D)), _full((8, D)), _full((8, D))],
        out_specs=[_row(D), _row(D), _full((1, D)), _full((8, D))],
        out_shape=[jax.ShapeDtypeStruct((t, D), F32), jax.ShapeDtypeStruct((t, D), BF16),
                   jax.ShapeDtypeStruct((1, D), F32), jax.ShapeDtypeStruct((8, D), F32)],
        compiler_params=_cp("arbitrary"),
    )(z1, y2, target, nw, modc, modx)


def _mid_bwd(dh2, dz, z, z1, y1, nw_post, nw_pre, modc, modx):
    t = z.shape[0]

    def body(dh_ref, dz_ref, z_ref, z1_ref, y_ref, wpo_ref, wpr_ref, mc_ref, mx_ref, dzo_ref, dy_ref, sm_ref):
        i = pl.program_id(0)
        is_ctx = i < NCT

        @pl.when(i == 0)
        def _():
            sm_ref[...] = jnp.zeros_like(sm_ref)

        dh = dh_ref[...].astype(F32)
        z1 = z1_ref[...]
        r = _rstd(z1)
        zr = z1 * r
        wpr = wpr_ref[...]
        n = zr * wpr
        _acc_mod(sm_ref, 0, is_ctx, _colsum(dh))
        _acc_mod(sm_ref, 2, is_ctx, _colsum(dh * n))
        dn = dh * (1.0 + _mod_row(mc_ref, mx_ref, 4, is_ctx))
        _acc_row(sm_ref, 6, _colsum(dn * zr))
        dz1 = dz_ref[...] + _rms_bwd(dn * wpr, z1, r)
        dzo_ref[...] = dz1
        y = y_ref[...].astype(F32)
        r1 = _rstd(y)
        yr = y * r1
        wpo = wpo_ref[...]
        g = _mod_row(mc_ref, mx_ref, 2, is_ctx)
        _acc_mod(sm_ref, 4, is_ctx, _colsum(dz1 * (yr * wpo)))
        dn1 = dz1 * g
        _acc_row(sm_ref, 7, _colsum(dn1 * yr))
        dy_ref[...] = _rms_bwd(dn1 * wpo, y, r1).astype(BF16)

    return pl.pallas_call(
        body, name="mid_bwd", grid=(t // TR,),
        in_specs=[_row(D)] * 5 + [_full((1, D)), _full((1, D)), _full((8, D)), _full((8, D))],
        out_specs=[_row(D), _row(D), _full((8, D))],
        out_shape=[jax.ShapeDtypeStruct((t, D), F32), jax.ShapeDtypeStruct((t, D), BF16),
                   jax.ShapeDtypeStruct((8, D), F32)],
        compiler_params=_cp("arbitrary"),
    )(dh2, dz, z, z1, y1, nw_post, nw_pre, modc, modx)


def _pre_bwd(dh1, dz, z, nw, modc, modx):
    t = z.shape[0]

    def body(dh_ref, dz_ref, z_ref, w_ref, mc_ref, mx_ref, dzo_ref, sm_ref):
        i = pl.program_id(0)
        is_ctx = i < NCT

        @pl.when(i == 0)
        def _():
            sm_ref[...] = jnp.zeros_like(sm_ref)

        dh = dh_ref[...].astype(F32)
        x = z_ref[...]
        r = _rstd(x)
        xr = x * r
        w = w_ref[...]
        _acc_mod(sm_ref, 0, is_ctx, _colsum(dh))
        _acc_mod(sm_ref, 2, is_ctx, _colsum(dh * (xr * w)))
        dn = dh * (1.0 + _mod_row(mc_ref, mx_ref, 1, is_ctx))
        _acc_row(sm_ref, 4, _colsum(dn * xr))
        dzo_ref[...] = dz_ref[...] + _rms_bwd(dn * w, x, r)

    return pl.pallas_call(
        body, name="pre_bwd", grid=(t // TR,),
        in_specs=[_row(D)] * 3 + [_full((1, D)), _full((8, D)), _full((8, D))],
        out_specs=[pl.BlockSpec((TR, D), lambda i: (jnp.maximum(i - NCT, 0), 0)), _full((8, D))],
        out_shape=[jax.ShapeDtypeStruct((t - CTX, D), F32), jax.ShapeDtypeStruct((8, D), F32)],
        compiler_params=_cp("arbitrary"),
    )(dh1, dz, z, nw, modc, modx)


def _merge_bwd(dm, p, u1, u2):
    t = dm.shape[0]

    def body(dm_ref, a0, a1, a2, b0, b1, b2, u1_ref, u2_ref, du1_ref, du2_ref, dg_ref):
        dm_ = dm_ref[...].astype(F32)
        s1 = _sig(_gate_window((a0, a1, a2)))
        s2 = _sig(_gate_window((b0, b1, b2)))
        du1_ref[...] = (dm_ * s1).astype(BF16)
        du2_ref[...] = (dm_ * s2).astype(BF16)
        dg_ref[:, :GW] = (dm_ * u1_ref[...].astype(F32) * s1 * (1.0 - s1)).astype(BF16)
        dg_ref[:, GW:] = (dm_ * u2_ref[...].astype(F32) * s2 * (1.0 - s2)).astype(BF16)

    return pl.pallas_call(
        body, name="merge_bwd", grid=(t // TR,),
        in_specs=[_row(GW)] + _gate_window_specs(GATE_HG0) + _gate_window_specs(GATE_GLA0) + [_row(GW), _row(GW)],
        out_specs=[_row(GW), _row(GW), _row(2 * GW)],
        out_shape=[jax.ShapeDtypeStruct((t, GW), BF16), jax.ShapeDtypeStruct((t, GW), BF16),
                   jax.ShapeDtypeStruct((t, 2 * GW), BF16)],
        compiler_params=_cp("parallel"),
    )(dm, p, p, p, p, p, p, u1, u2)


def _post_bwd(dy_hg, dy_gla, o_fw, o_bw, p, onw):
    t = o_fw.shape[0]

    def body(d1_ref, d2_ref, of_ref, ob_ref, g1_ref, g2_ref, w_ref, do_ref, dg_ref, sm_ref):
        @pl.when(pl.program_id(0) == 0)
        def _():
            sm_ref[...] = jnp.zeros_like(sm_ref)

        for h in range(NH):
            sl = slice(h * DH, (h + 1) * DH)
            gs = slice((h % (NH // 2)) * DH, (h % (NH // 2) + 1) * DH)
            g_ref, d_ref = (g1_ref, d1_ref) if h < NH // 2 else (g2_ref, d2_ref)
            o = of_ref[:, sl] + ob_ref[:, sl]
            r = _rstd(o)
            orr = o * r
            w = w_ref[:, sl]
            gt = g_ref[:, gs].astype(F32)
            dy = d_ref[:, gs].astype(F32)
            dg_ref[:, sl] = (dy * (orr * w) * _dsilu(gt)).astype(BF16)
            dn = dy * _silu(gt)
            sm_ref[0:1, sl] += _colsum(dn * orr)
            do_ref[:, sl] = _rms_bwd(dn * w, o, r)

    return pl.pallas_call(
        body, name="post_bwd", grid=(t // TR,),
        in_specs=[_row(HW), _row(HW), _row(D), _row(D), _rowcol(HW, MAIN0 // HW + 4), _rowcol(HW, MAIN0 // HW + 8),
                  _full((1, D))],
        out_specs=[_row(D), _row(D), _full((8, D))],
        out_shape=[jax.ShapeDtypeStruct((t, D), F32), jax.ShapeDtypeStruct((t, D), BF16),
                   jax.ShapeDtypeStruct((8, D), F32)],
        compiler_params=_cp("arbitrary"),
    )(dy_hg, dy_gla, o_fw, o_bw, p, p, onw)


def _gates_bwd(p, hg_lb, wgk, bgk, dgm, dgo, dq_f, dq_b, dv_f, dv_b, dk_f, dk_b, dg_f, dg_b):
    t = p.shape[0]
    seg = lambda j: _rowcol(HW, MAIN0 // HW + j)

    def body(hq_ref, hf_ref, hb_ref, lr_ref, lb_ref, wgk_ref, bgk_ref, dgm_ref, dgo_ref,
             dqf_ref, dqb_ref, dvf_ref, dvb_ref, dkf_ref, dkb_ref, dgf_ref, dgb_ref,
             dp_ref, dlb_ref, dw_ref, db_ref):
        @pl.when(pl.program_id(0) == 0)
        def _():
            dlb_ref[...] = jnp.zeros_like(dlb_ref)
            dw_ref[...] = jnp.zeros_like(dw_ref)
            db_ref[...] = jnp.zeros_like(db_ref)

        c0 = MAIN0

        def put(j, val):
            dp_ref[:, c0 + j * HW:c0 + (j + 1) * HW] = val.astype(BF16)

        dq = dqf_ref[...].astype(F32) + dqb_ref[...].astype(F32)
        dv = dvf_ref[...].astype(F32) + dvb_ref[...].astype(F32)
        put(0, dq[:, :HW] * _dsilu(hq_ref[...].astype(F32)))
        put(1, dv[:, :HW])
        put(5, dq[:, HW:] * (DH ** -0.5))
        put(7, dv[:, HW:])
        put(6, dkf_ref[:, HW:].astype(F32) + dkb_ref[:, HW:].astype(F32))
        dp_ref[:, c0 + 4 * HW:c0 + 5 * HW] = dgo_ref[:, :HW]
        dp_ref[:, c0 + 8 * HW:c0 + 9 * HW] = dgo_ref[:, HW:]
        lr = lr_ref[...].astype(BF16)
        xg = _dot(lr, wgk_ref[...], NN) + bgk_ref[...]
        dxg = []
        for d, (raw_ref, dk_ref, dg_ref) in enumerate(((hf_ref, dkf_ref, dgf_ref), (hb_ref, dkb_ref, dgb_ref))):
            lbd = _hg_lb(lb_ref, d)
            s = _sig(raw_ref[...].astype(F32))
            f = lbd + (1.0 - lbd) * s
            df = dg_ref[:, :HW] / f - dk_ref[:, :HW].astype(F32)
            put(2 + d, df * (1.0 - lbd) * s * (1.0 - s))
            dlb_ref[d:d + 1, :] += _colsum(df * (1.0 - s)) * (lbd * (1.0 - lbd))
            dxg.append(dg_ref[:, HW:] * (1.0 / GLA_NORM) * _sig(-xg[:, d * HW:(d + 1) * HW]))
        dxg = jnp.concatenate(dxg, axis=1)
        db_ref[0:1, :] += _colsum(dxg)
        dxg_b = dxg.astype(BF16)
        dw_ref[...] += _dot(lr, dxg_b, TN)
        dlr = _dot(dxg_b, wgk_ref[...], NT)
        dp_ref[:, LR0:LR0 + DH] = (dlr + dgm_ref[:, :DH].astype(F32)).astype(BF16)
        dp_ref[:, LR0 + DH:GATE_GLA0] = dgm_ref[:, DH:D]
        dp_ref[:, GATE_GLA0:GATE_GLA0 + DH] = dgm_ref[:, D:GW] + dgm_ref[:, GW:GW + DH]
        dp_ref[:, GATE_GLA0 + DH:GATE_GLA0 + GW] = dgm_ref[:, GW + DH:]
        dp_ref[:, GATE_GLA0 + GW:] = jnp.zeros((TR, W_IN_COLS - GATE_GLA0 - GW), BF16)

    return pl.pallas_call(
        body, name="gates_bwd", grid=(t // TR,),
        in_specs=[seg(0), seg(2), seg(3), _rowcol(DH, LR0 // DH), _full((2, 2, HW)), _full((DH, D)), _full((1, D)),
                  _row(2 * GW), _row(D)] + [_row(D)] * 8,
        out_specs=[_row(W_IN_COLS), _full((8, HW)), _full((DH, D)), _full((8, D))],
        out_shape=[jax.ShapeDtypeStruct((t, W_IN_COLS), BF16), jax.ShapeDtypeStruct((8, HW), F32),
                   jax.ShapeDtypeStruct((DH, D), F32), jax.ShapeDtypeStruct((8, D), F32)],
        compiler_params=_cp("arbitrary"),
    )(p, p, p, p, hg_lb, wgk, bgk, dgm, dgo, dq_f, dq_b, dv_f, dv_b, dk_f, dk_b, dg_f, dg_b)


def _scan_consts(rev):
    r = lax.broadcasted_iota(jnp.int32, (CHUNK, CHUNK), 0)
    u = lax.broadcasted_iota(jnp.int32, (CHUNK, CHUNK), 1)
    rp = lax.broadcasted_iota(jnp.int32, (CHUNK, 1), 0)
    if rev:
        r, u, rp = CHUNK - 1 - r, CHUNK - 1 - u, CHUNK - 1 - rp
    tri = jnp.where(u <= r, 1.0, 0.0).astype(F32)
    tri_t = jnp.where(r <= u, 1.0, 0.0).astype(F32)
    lv = []
    for b in LEVELS:
        sh = b.bit_length() - 1
        pair = ((r >> sh) == (u >> sh) + 1) & (((u >> sh) & 1) == 0)
        pair_t = ((u >> sh) == (r >> sh) + 1) & (((r >> sh) & 1) == 0)
        tside = ((rp >> sh) & 1) == 1
        lv.append((pair, pair_t, tside))
    bd = LEVELS[-1].bit_length() - 1
    diag = ((r >> bd) == (u >> bd)) & (u <= r)
    diag_t = ((r >> bd) == (u >> bd)) & (r <= u)
    return tri, tri_t, lv, diag, diag_t


def _row_of(pos, rev):
    return CHUNK - 1 - pos if rev else pos


def _chunk_terms(cum, b_scr, consts, rev):
    _, _, lv, _, _ = consts
    terms = []
    for b, (_, _, tside) in zip(LEVELS, lv):
        pieces = []
        for j in range(CHUNK // (2 * b)):
            row = _row_of(2 * b * j + b - 1, rev)
            pieces.append(jnp.broadcast_to(b_scr[row:row + 1, :], (2 * b, DH)))
        if rev:
            pieces = pieces[::-1]
        bnd = pieces[0] if len(pieces) == 1 else jnp.concatenate(pieces, axis=0)
        w = jnp.exp(jnp.minimum(jnp.where(tside, cum - bnd, bnd - cum), 0.0))
        wq = jnp.where(tside, w, 0.0)
        wk = jnp.where(tside, 0.0, w)
        terms.append((wq, wk))
    b = LEVELS[-1]
    pieces = []
    for j in range(CHUNK // b):
        if j == 0:
            pieces.append(jnp.zeros((b, DH), F32))
        else:
            row = _row_of(b * j - 1, rev)
            pieces.append(jnp.broadcast_to(b_scr[row:row + 1, :], (b, DH)))
    if rev:
        pieces = pieces[::-1]
    start = jnp.concatenate(pieces, axis=0)
    wq = jnp.exp(jnp.minimum(cum - start, 0.0))
    wk = jnp.exp(jnp.minimum(start - cum, EXP_CLAMP))
    terms.append((wq, wk))
    return terms


def _run_staged(units):
    live = list(units)
    while live:
        nxt = []
        for u in live:
            try:
                next(u)
                nxt.append(u)
            except StopIteration:
                pass
        live = nxt


SCAN_TB = 256
SCAN_CB = SCAN_TB // CHUNK


def _block_order(i, ntb, rev):
    nctx = CTX // SCAN_TB
    if not rev:
        return i
    return jnp.where(i < nctx, nctx - 1 - i, ntb - 1 - (i - nctx))


def _chunk_in_block(j, rev):
    return SCAN_CB - 1 - j if rev else j


def _scan_fwd(q, k, v, g, rev):
    t = q.shape[0]
    nc = t // CHUNK
    hpb = SCAN_HEADS_FWD

    def body(q_ref, k_ref, v_ref, g_ref, o_ref, st_ref, s_scr, b_scr):
        consts = _scan_consts(rev)
        _, _, lv, diag, _ = consts
        masks = [pair for pair, _, _ in lv] + [diag]

        @pl.when(pl.program_id(1) == 0)
        def _():
            s_scr[...] = jnp.zeros_like(s_scr)

        tri = consts[0]
        state = {hh: s_scr[hh] for hh in range(hpb)}

        def unit(hh, j):
            sl = slice(hh * DH, (hh + 1) * DH)
            c = _chunk_in_block(j, rev)
            rows = slice(c * CHUNK, (c + 1) * CHUNK)
            b_ref = b_scr.at[hh * SCAN_CB + j]
            qc, kc, vc, gc = q_ref[rows, sl], k_ref[rows, sl], v_ref[rows, sl], g_ref[rows, sl]
            cum = _split_dot(tri, gc)
            b_ref[...] = cum
            yield
            terms = _chunk_terms(cum, b_ref, consts, rev)
            ops = [((qc * wq).astype(BF16), (kc * wk).astype(BF16)) for wq, wk in terms]
            tot = _colsum(gc)
            qe = (qc * jnp.exp(cum)).astype(BF16)
            ke = (kc * jnp.exp(tot - cum)).astype(BF16)
            vb = vc.astype(BF16)
            yield
            scs = [_dot(qt, kt, NT) for qt, kt in ops]
            kv = _dot(vb, ke, TN)
            yield
            a = jnp.zeros((CHUNK, CHUNK), F32)
            for sc, m in zip(scs, masks):
                a = a + jnp.where(m, sc, 0.0)
            o_intra = _dot(a.astype(BF16), vb, NN)
            yield
            st = state[hh]
            st_ref[hh, c] = st
            o_ref[rows, sl] = o_intra + _dot(qe, st.astype(BF16), NT)
            state[hh] = st * jnp.exp(tot) + kv
            yield

        _run_staged([unit(hh, j) for hh in range(hpb) for j in range(SCAN_CB)])
        for hh in range(hpb):
            s_scr[hh] = state[hh]

    ntb = t // SCAN_TB
    col = pl.BlockSpec((SCAN_TB, hpb * DH), lambda h, i: (_block_order(i, ntb, rev), h))
    return pl.pallas_call(
        body, name="scan_fwd_" + ("bw" if rev else "fw"), grid=(NH // hpb, ntb),
        in_specs=[col] * 4,
        out_specs=[col, pl.BlockSpec((hpb, SCAN_CB, DH, DH), lambda h, i: (h, _block_order(i, ntb, rev), 0, 0))],
        out_shape=[jax.ShapeDtypeStruct((t, D), F32), jax.ShapeDtypeStruct((NH, nc, DH, DH), F32)],
        scratch_shapes=[pltpu.VMEM((hpb, DH, DH), F32), pltpu.VMEM((hpb * SCAN_CB, CHUNK, DH), F32)],
        compiler_params=_cp("parallel", "arbitrary"),
    )(q, k, v, g)


def _scan_bwd(q, k, v, g, do, states, rev):
    t = q.shape[0]
    nc = t // CHUNK
    hpb = SCAN_HEADS_BWD

    def body(q_ref, k_ref, v_ref, g_ref, do_ref, st_ref, dq_ref, dk_ref, dv_ref, dg_ref, ds_scr, b_scr):
        consts = _scan_consts(rev)
        _, tri_t, lv, diag, diag_t = consts
        masks = [(pair, pair_t) for pair, pair_t, _ in lv] + [(diag, diag_t)]
        @pl.when(pl.program_id(1) == 0)
        def _():
            ds_scr[...] = jnp.zeros_like(ds_scr)

        tri = consts[0]
        dstate = {hh: ds_scr[hh] for hh in range(hpb)}

        def unit(hh, jj):
            sl = slice(hh * DH, (hh + 1) * DH)
            c = _chunk_in_block(SCAN_CB - 1 - jj, rev)
            rows = slice(c * CHUNK, (c + 1) * CHUNK)
            b_ref = b_scr.at[hh * SCAN_CB + jj]
            qc, kc, vc, gc = q_ref[rows, sl], k_ref[rows, sl], v_ref[rows, sl], g_ref[rows, sl]
            dob = do_ref[rows, sl].astype(BF16)
            vb = vc.astype(BF16)
            cum = _split_dot(tri, gc)
            b_ref[...] = cum
            da = _dot(dob, vb, NT)
            da_t = _dot(vb, dob, NT)
            yield
            terms = _chunk_terms(cum, b_ref, consts, rev)
            ops = [((qc * wq).astype(BF16), (kc * wk).astype(BF16)) for wq, wk in terms]
            tot = _colsum(gc)
            e_tot = jnp.exp(tot)
            e_b = jnp.exp(cum)
            e_t = jnp.exp(tot - cum)
            qeb = (qc * e_b).astype(BF16)
            keb = (kc * e_t).astype(BF16)
            dal = [(jnp.where(m, da, 0.0).astype(BF16), jnp.where(m_t, da_t, 0.0).astype(BF16)) for m, m_t in masks]
            yield
            ats = [_dot(ktb, qtb, NT) for qtb, ktb in ops]
            dqts = [_dot(d, ktb, NN) for (d, _), (_, ktb) in zip(dal, ops)]
            dkts = [_dot(d_t, qtb, NN) for (_, d_t), (qtb, _) in zip(dal, ops)]
            qd = _dot(dob, qeb, TN)
            yield
            a_t = jnp.zeros((CHUNK, CHUNK), F32)
            dq = jnp.zeros((CHUNK, DH), F32)
            dk = jnp.zeros((CHUNK, DH), F32)
            db = jnp.zeros((CHUNK, DH), F32)
            for at, dqt, dkt, (wq, wk), (qtb, ktb), (_, m_t) in zip(ats, dqts, dkts, terms, ops, masks):
                a_t = a_t + jnp.where(m_t, at, 0.0)
                dq = dq + dqt * wq
                dk = dk + dkt * wk
                db = db + dqt * qtb.astype(F32) - dkt * ktb.astype(F32)
            dv_intra = _dot(a_t.astype(BF16), dob, NN)
            st = st_ref[hh, c]
            stb = st.astype(BF16)
            dqe = _dot(dob, stb, NN)
            yield
            dst = dstate[hh]
            dstb = dst.astype(BF16)
            dstate[hh] = dst * e_tot + qd
            dv_ref[rows, sl] = (dv_intra + _dot(keb, dstb, NT)).astype(BF16)
            dke = _dot(vb, dstb, NN)
            yield
            qe = qeb.astype(F32)
            ke = keb.astype(F32)
            dq_ref[rows, sl] = (dq + dqe * e_b).astype(BF16)
            dk_ref[rows, sl] = (dk + dke * e_t).astype(BF16)
            db = db + dqe * qe - dke * ke
            dtot = _colsum(dstb.astype(F32) * stb.astype(F32)) * e_tot + _colsum(dke * ke)
            dg_ref[rows, sl] = _split_dot(tri_t, db) + dtot
            yield

        _run_staged([unit(hh, jj) for hh in range(hpb) for jj in range(SCAN_CB)])
        for hh in range(hpb):
            ds_scr[hh] = dstate[hh]

    ntb = t // SCAN_TB
    blk = lambda i: _block_order(ntb - 1 - i, ntb, rev)
    col = pl.BlockSpec((SCAN_TB, hpb * DH), lambda h, i: (blk(i), h))
    out = jax.ShapeDtypeStruct((t, D), F32)
    outb = jax.ShapeDtypeStruct((t, D), BF16)
    return pl.pallas_call(
        body, name="scan_bwd_" + ("bw" if rev else "fw"), grid=(NH // hpb, ntb),
        in_specs=[col] * 5 + [pl.BlockSpec((hpb, SCAN_CB, DH, DH), lambda h, i: (h, blk(i), 0, 0))],
        out_specs=[col] * 4,
        out_shape=[outb] * 3 + [out],
        scratch_shapes=[pltpu.VMEM((hpb, DH, DH), F32), pltpu.VMEM((hpb * SCAN_CB, CHUNK, DH), F32)],
        compiler_params=_cp("parallel", "arbitrary"),
    )(q, k, v, g, do, states)


W_IN_GRAD_CHUNKS = (("a", (0, 512)), ("b", (0, 128)), ("b", (128, 512)))
W_IN_REF = 6688


def _layout_w_in(w):
    return jnp.pad(w, ((0, 0), (0, W_IN_COLS - W_IN_REF)))


def _unlayout_w_in(d):
    return d[:, :W_IN_REF]


def _gate_cols(w):
    return jnp.pad(w, ((0, 0), (GOFF, GW - GOFF - D)))


def _gate_rows(w):
    return jnp.pad(w, ((GOFF, GW - GOFF - D), (0, 0)))


def _layout_wgk(w):
    r = w.shape[1]
    top = jnp.concatenate([w[0], jnp.zeros_like(w[0])], axis=1)
    bot = jnp.concatenate([jnp.zeros_like(w[1]), w[1]], axis=1)
    return jnp.concatenate([top, bot, jnp.zeros((DH - 2 * r, D), w.dtype)], axis=0)


def _unlayout_wgk(d, r=16):
    return jnp.stack([d[:r, :HW], d[r:2 * r, HW:]])


def _local_step(z, target, modc, modx, norms, onw, hg_lb, wgk, bgk, get_w_in, get_mix, get_ffn, send):
    n_pre1, n_post1, n_pre2, n_post2 = norms
    t = z.shape[0]
    tm = 768 if t % 768 == 0 else 256
    h1 = _prenorm(z, n_pre1, modc, modx, 0, 1, "prenorm1")
    w_in = get_w_in(h1)
    p = _matmul(h1, w_in, NN, BF16, "mm_in", tm, 1024, D)
    q, v, k_f, k_b, g_f, g_b = _gates_fwd(p, hg_lb, wgk, bgk)
    o_f, st_f = _scan_fwd(q, k_f, v, g_f, False)
    o_b, st_b = _scan_fwd(q, k_b, v, g_b, True)
    y = _post_fwd(o_f, o_b, p, onw)
    w_br_hg, w_br_gla, w_out = get_mix(y)
    u1 = _matmul(y, w_br_hg, NN, BF16, "mm_br_hg", tm, GW, HW, a_off=0)
    u2 = _matmul(y, w_br_gla, NN, BF16, "mm_br_gla", tm, GW, HW, a_off=1)
    merged = _merge_fwd(p, u1, u2)
    y1 = _matmul(merged, w_out, NN, BF16, "mm_out", tm, 512, GW)
    z1, h2 = _mid_fwd(z, y1, n_post1, n_pre2, modc, modx)
    w_gu_t, w_down = get_ffn(h2)
    uv = _matmul(h2, w_gu_t, NT, BF16, "mm_gu", tm, D_FF // 2, D)
    act = _swiglu_fwd(uv)
    y2 = _matmul(act, w_down, NN, BF16, "mm_down", tm, 512, D_FF)
    dz, dy2, loss_vec, sm_final = _final(z1, y2, target, n_post2, modc, modx)
    dact = _matmul(dy2, w_down, NT, BF16, "mm_down_dx", tm, D_FF // 2, D)
    d_w_down = _matmul(act, dy2, TN, BF16, "mm_down_dw", D_FF // 2, 512, t)
    duv = _swiglu_bwd(uv, dact)
    dh2 = _matmul(duv, w_gu_t, NN, BF16, "mm_gu_dx", tm, 512, D_FF)
    d_w_gate_t = _matmul(duv, h2, TN, BF16, "mm_gate_dw", D_FF // 2, 512, t, a_off=0, m_out=D_FF)
    d_w_up_t = _matmul(duv, h2, TN, BF16, "mm_up_dw", D_FF // 2, 512, t, a_off=2, m_out=D_FF)
    dh2 = send(("w_down", "w_gate_t", "w_up_t"), (d_w_down, d_w_gate_t, d_w_up_t), dh2)
    dz, dy1, sm_mid = _mid_bwd(dh2, dz, z, z1, y1, n_post1, n_pre2, modc, modx)
    dmerged = _matmul(dy1, w_out, NT, BF16, "mm_out_dx", tm, GW, D)
    d_w_out = _matmul(merged, dy1, TN, BF16, "mm_out_dw", GW, 512, t)
    du1, du2, dgm = _merge_bwd(dmerged, p, u1, u2)
    dy_hg = _matmul(du1, w_br_hg, NT, BF16, "mm_br_hg_dx", tm, HW, GW)
    dy_gla = _matmul(du2, w_br_gla, NT, BF16, "mm_br_gla_dx", tm, HW, GW)
    d_w_br_hg = _matmul(y, du1, TN, BF16, "mm_br_hg_dw", HW, GW, t, a_off=0, m_out=HW)
    d_w_br_gla = _matmul(y, du2, TN, BF16, "mm_br_gla_dw", HW, GW, t, a_off=1, m_out=HW)
    dy_hg = send(("w_out", "w_br_hg", "w_br_gla"), (d_w_out, d_w_br_hg, d_w_br_gla), dy_hg)
    do, dgo, sm_post = _post_bwd(dy_hg, dy_gla, o_f, o_b, p, onw)
    dq_f, dk_f, dv_f, dg_f = _scan_bwd(q, k_f, v, g_f, do, st_f, False)
    dq_b, dk_b, dv_b, dg_b = _scan_bwd(q, k_b, v, g_b, do, st_b, True)
    dp, d_lb, d_wgk, d_bgk = _gates_bwd(p, hg_lb, wgk, bgk, dgm, dgo, dq_f, dq_b, dv_f, dv_b, dk_f, dk_b, dg_f, dg_b)
    d_w_in_a = _matmul(h1, dp, TN, BF16, "mm_in_dw_a", 512, 512, t, a_off=0, m_out=D // 2)
    dp = send(("w_in_a",), (d_w_in_a,), dp)
    d_w_in_b = _matmul(h1, dp, TN, BF16, "mm_in_dw_b", 512, 512, t, a_off=1, m_out=D // 2)
    dp = send(("w_in_b",), (d_w_in_b,), dp)
    dh1 = _matmul(dp, w_in, NT, BF16, "mm_in_dx", tm, 512, W_IN_COLS // 2)
    grad_x, sm_pre = _pre_bwd(dh1, dz, z, n_pre1, modc, modx)
    return dict(loss_vec=loss_vec, grad_x=grad_x, sm_final=sm_final, sm_mid=sm_mid, sm_post=sm_post, sm_pre=sm_pre,
                d_lb=d_lb, d_wgk=d_wgk, d_bgk=d_bgk)


MESH = pl.DeviceIdType.MESH
ANY = pl.BlockSpec(memory_space=pl.ANY)
N_REL = N_DEV - 1


def _place():
    return lax.axis_index("x"), lax.axis_index("y"), lax.axis_index("c")


def _slot(p):
    return 4 * p[0] + 2 * p[1] + p[2]


def _all_gather(arrays, name):
    n = len(arrays)

    def body(*refs):
        ins, outs = refs[:n], refs[n:2 * n]
        send_sems, recv_sems, local_sems = refs[2 * n:]
        x, y, c = _place()
        me, sibling = (x, y, c), (x, y, 1 - c)
        chips = [(1 - x, y), (x, 1 - y), (1 - x, 1 - y)]

        def copy(a, k, block, to, src=None):
            dst = outs[a].at[_slot(block)]
            return pltpu.make_async_remote_copy(
                src_ref=dst if src is None else src, dst_ref=dst,
                send_sem=send_sems.at[N_REL * a + k], recv_sem=recv_sems.at[N_REL * a + k],
                device_id=to, device_id_type=MESH)

        mine = [pltpu.make_async_copy(ins[a], outs[a].at[_slot(me)], local_sems.at[a]) for a in range(n)]
        for cp in mine:
            cp.start()
        first = []
        for a in range(n):
            first.append(copy(a, 0, me, sibling, src=ins[a]))
            first += [copy(a, 1 + j, me, (*chip, c), src=ins[a]) for j, chip in enumerate(chips)]
        for cp in first:
            cp.start()
        passed = []
        for j, chip in enumerate(chips):
            for a in range(n):
                copy(a, 1 + j, (*chip, c), me).wait_recv()
                fwd = copy(a, 4 + j, (*chip, c), sibling)
                fwd.start()
                passed.append(fwd)
        for a in range(n):
            copy(a, 0, sibling, me).wait_recv()
        for j, chip in enumerate(chips):
            for a in range(n):
                copy(a, 4 + j, (*chip, 1 - c), me).wait_recv()
        for cp in first + passed:
            cp.wait_send()
        for cp in mine:
            cp.wait()

    return pl.pallas_call(
        body, name=name,
        in_specs=[ANY] * n, out_specs=[ANY] * n,
        out_shape=[jax.ShapeDtypeStruct((N_DEV,) + a.shape, a.dtype) for a in arrays],
        scratch_shapes=[pltpu.SemaphoreType.DMA((N_REL * n,)), pltpu.SemaphoreType.DMA((N_REL * n,)),
                        pltpu.SemaphoreType.DMA((n,))],
    )(*arrays)


def _exchange(arrays, name):
    n = len(arrays)

    def body(*refs):
        ins, outs = refs[:n], refs[n:2 * n]
        send_sems, recv_sems, local_sems = refs[2 * n:]
        x, y, c = _place()
        me = _slot((x, y, c))
        mine = [pltpu.make_async_copy(ins[a].at[me], outs[a].at[me], local_sems.at[a]) for a in range(n)]
        for cp in mine:
            cp.start()
        copies = []
        for a in range(n):
            for k in range(1, N_DEV):
                flip = lambda v, bit: 1 - v if bit else v
                peer = (flip(x, k & 4), flip(y, k & 2), flip(c, k & 1))
                copies.append(pltpu.make_async_remote_copy(
                    src_ref=ins[a].at[_slot(peer)], dst_ref=outs[a].at[me],
                    send_sem=send_sems.at[N_REL * a + k - 1], recv_sem=recv_sems.at[N_REL * a + k - 1],
                    device_id=peer, device_id_type=MESH))
                copies[-1].start()
        i = 0
        for a in range(n):
            for k in range(1, N_DEV):
                flip = lambda v, bit: 1 - v if bit else v
                peer = (flip(x, k & 4), flip(y, k & 2), flip(c, k & 1))
                pltpu.make_async_remote_copy(
                    src_ref=ins[a].at[_slot(peer)], dst_ref=outs[a].at[_slot(peer)],
                    send_sem=send_sems.at[N_REL * a + k - 1], recv_sem=recv_sems.at[N_REL * a + k - 1],
                    device_id=peer, device_id_type=MESH).wait_recv()
                i += 1
        for cp in copies:
            cp.wait_send()
        for cp in mine:
            cp.wait()

    return pl.pallas_call(
        body, name=name,
        in_specs=[ANY] * n, out_specs=[ANY] * n,
        out_shape=[jax.ShapeDtypeStruct(a.shape, a.dtype) for a in arrays],
        scratch_shapes=[pltpu.SemaphoreType.DMA((N_REL * n,)), pltpu.SemaphoreType.DMA((N_REL * n,)),
                        pltpu.SemaphoreType.DMA((n,))],
    )(*arrays)


HBM = pl.BlockSpec(memory_space=pltpu.HBM)
SEM = pl.BlockSpec(memory_space=pltpu.SEMAPHORE)
EFFECT = pltpu.SideEffectType.DATAFLOW_SIDE_EFFECTING


def _peer_of(x, y, c, k):
    flip = lambda v, bit: 1 - v if bit else v
    return flip(x, k & 4), flip(y, k & 2), flip(c, k & 1)


def _view_whole(src, slot):
    return src


def _view_near(src, slot):
    return src


_view_near.peers = (1, 2, 4, 6)


def _view_block(src, slot):
    return src.at[slot]


W_IN_SHARD = W_IN_REF // N_DEV


def _view_window(rows):
    def view(src, slot):
        col0 = pl.multiple_of((W_IN_SHARD * slot // DH) * DH, DH)
        return src.at[pl.ds(rows[0], rows[1] - rows[0]), pl.ds(col0, D)]
    return view


def _split_copies(view, srcs, lands, send_sems, recv_sems, local_sems):
    x, y, c = _place()
    me = _slot((x, y, c))
    local, sends, waits = [], [], []
    for a, (src, land) in enumerate(zip(srcs, lands)):
        local.append(pltpu.make_async_copy(view(src, me), land.at[me], local_sems.at[a]))
        for k in getattr(view, "peers", range(1, N_DEV)):
            peer = _peer_of(x, y, c, k)
            mine = view(src, _slot(peer))
            sems = dict(send_sem=send_sems.at[N_REL * a + k - 1], recv_sem=recv_sems.at[N_REL * a + k - 1],
                        device_id=peer, device_id_type=MESH)
            sends.append(pltpu.make_async_remote_copy(src_ref=mine, dst_ref=land.at[me], **sems))
            waits.append(pltpu.make_async_remote_copy(src_ref=mine, dst_ref=land.at[_slot(peer)], **sems))
    return local, sends, waits


def _split_start(view, land_shapes, srcs, name, after):
    n = len(srcs)
    lands = [lax.empty(shp, s.dtype) for shp, s in zip(land_shapes, srcs)]

    def body(*refs):
        src_refs, land_refs = refs[:n], refs[n:2 * n]
        send_sems, recv_sems, local_sems = refs[2 * n + 1:2 * n + 4]
        token = refs[-1]
        local, sends, _ = _split_copies(view, src_refs, land_refs, send_sems, recv_sems, local_sems)
        for cp in local + sends:
            cp.start()
        token[...] = jnp.zeros_like(token)

    hbm = lambda a: pltpu.with_memory_space_constraint(a, pltpu.HBM)
    out = pl.pallas_call(
        body, name=name,
        out_shape=(pltpu.SemaphoreType.DMA((N_REL * n,)), pltpu.SemaphoreType.DMA((N_REL * n,)),
                   pltpu.SemaphoreType.DMA((n,)),
                   *[pltpu.HBM(s.shape, s.dtype) for s in srcs], *[pltpu.HBM(l.shape, l.dtype) for l in lands],
                   jax.ShapeDtypeStruct((8, DH), F32)),
        in_specs=[HBM] * (2 * n) + [ANY],
        out_specs=(SEM, SEM, SEM, *([HBM] * (2 * n)), pl.BlockSpec(memory_space=pltpu.VMEM)),
        input_output_aliases={i: 3 + i for i in range(2 * n)},
        compiler_params=pltpu.CompilerParams(has_side_effects=EFFECT),
    )(*[hbm(s) for s in srcs], *[hbm(l) for l in lands], after)
    handle = dict(view=view, n=n, sems=out[:3], srcs=list(out[3:3 + n]), lands=list(out[3 + n:3 + 2 * n]))
    return handle, out[-1]


def _split_wait(handle, name, after, srcs=None):
    view, n, sems, lands = handle["view"], handle["n"], handle["sems"], handle["lands"]
    srcs = handle["srcs"] if srcs is None else srcs
    afters = list(after) if isinstance(after, (list, tuple)) else [after]

    def body(*refs):
        src_refs, land_refs = refs[:n], refs[n:2 * n]
        send_sems, recv_sems, local_sems = refs[2 * n:2 * n + 3]
        local, _, waits = _split_copies(view, src_refs, land_refs, send_sems, recv_sems, local_sems)
        for cp in waits:
            cp.wait_send()
            cp.wait_recv()
        for cp in local:
            cp.wait()

    out = pl.pallas_call(
        body, name=name,
        out_shape=(*[pltpu.HBM(s.shape, s.dtype) for s in srcs], *[pltpu.HBM(l.shape, l.dtype) for l in lands]),
        in_specs=[HBM] * (2 * n) + [SEM, SEM, SEM] + [ANY] * len(afters),
        out_specs=tuple([HBM] * (2 * n)),
        input_output_aliases={i: i for i in range(2 * n)},
        compiler_params=pltpu.CompilerParams(has_side_effects=EFFECT),
    )(*srcs, *lands, *sems, *afters)
    handle["srcs"] = list(out[:n])
    return list(out[n:])


def _tie(x, token, name):
    def body(x_ref, t_ref, o_ref):
        pass

    return pl.pallas_call(
        body, name=name, out_shape=jax.ShapeDtypeStruct(x.shape, x.dtype),
        in_specs=[ANY, ANY], out_specs=ANY, input_output_aliases={0: 0},
    )(x, token)


def _forward_to_sibling(land, name):
    def body(land_ref, out_ref, send_sems, recv_sems):
        x, y, c = _place()
        sibling = (x, y, 1 - c)
        chips = [(1 - x, y), (x, 1 - y), (1 - x, 1 - y)]

        def copy(j, core):
            blk = _slot((*chips[j], core))
            return pltpu.make_async_remote_copy(src_ref=land_ref.at[blk], dst_ref=out_ref.at[blk],
                                                send_sem=send_sems.at[j], recv_sem=recv_sems.at[j],
                                                device_id=sibling, device_id_type=MESH)

        sends = [copy(j, c) for j in range(3)]
        for cp in sends:
            cp.start()
        for j in range(3):
            copy(j, 1 - c).wait_recv()
        for cp in sends:
            cp.wait_send()

    return pl.pallas_call(
        body, name=name, in_specs=[ANY], out_specs=ANY, input_output_aliases={0: 0},
        out_shape=jax.ShapeDtypeStruct(land.shape, land.dtype),
        scratch_shapes=[pltpu.SemaphoreType.DMA((3,)), pltpu.SemaphoreType.DMA((3,))],
    )(land)


def _mod_fwd(a, w, b):
    def body(a_ref, w_ref, b_ref, o_ref):
        o_ref[...] = _dot(_silu(a_ref[...]), w_ref[...], NN, precision=HI) + b_ref[...]

    return pl.pallas_call(
        body, name="mod_fwd", out_shape=jax.ShapeDtypeStruct((a.shape[0], w.shape[1]), F32),
        compiler_params=pltpu.CompilerParams(vmem_limit_bytes=VMEM_LIMIT),
    )(a, w, b)


def _mod_bwd(a, d, w):
    def body(a_ref, d_ref, w_ref, dw_ref, dc_ref):
        av = a_ref[...]
        dv = d_ref[...]
        dw_ref[...] = _dot(_silu(av), dv, TN, precision=HI)
        da = _dot(dv[0:8, :], w_ref[...], NT, precision=HI) * _dsilu(av[0:8, :])
        row = lax.broadcasted_iota(jnp.int32, da.shape, 0)
        dc_ref[...] = jnp.where(row == 0, da, 0.0)

    return pl.pallas_call(
        body, name="mod_bwd",
        out_shape=[jax.ShapeDtypeStruct(w.shape, F32), jax.ShapeDtypeStruct((8, w.shape[0]), F32)],
        compiler_params=pltpu.CompilerParams(vmem_limit_bytes=VMEM_LIMIT),
    )(a, d, w)


def _sum_devices(g):
    def body(g_ref, o_ref):
        acc = g_ref[0]
        for i in range(1, g.shape[0]):
            acc = acc + g_ref[i]
        o_ref[...] = acc

    return pl.pallas_call(body, name="sum_devices_%d" % g.shape[1],
                          out_shape=jax.ShapeDtypeStruct(g.shape[1:], F32))(g)


def _sum_windows(g, name):
    n, r, c = g.shape
    tr = 128

    def body(g_ref, o_ref):
        x, y, cc = _place()
        lane0 = (W_IN_SHARD * _slot((x, y, cc))) % DH
        acc = g_ref[0].astype(F32)
        for i in range(1, n):
            acc = acc + g_ref[i].astype(F32)
        o_ref[...] = pltpu.roll(acc, (c - lane0) % c, 1).T

    return pl.pallas_call(
        body, name=name, grid=(r // tr,),
        in_specs=[pl.BlockSpec((n, tr, c), lambda i: (0, i, 0))],
        out_specs=pl.BlockSpec((c, tr), lambda i: (0, i)),
        out_shape=jax.ShapeDtypeStruct((c, r), F32),
        compiler_params=_cp("parallel"),
    )(g)


def _adam_rows(r, c, n):
    budget = 6 * 1024 * 1024
    best = None
    for tr in range(16, r + 1, 16):
        if r % tr == 0 and tr * c * (2 * n + 28) <= budget:
            best = tr
    return best if best is not None else r


def _adamw(g, w, m, v, name):
    n, r, c = g.shape
    tr = _adam_rows(r, c, n)
    bc1 = 1.0 - ADAM_B1 ** ADAM_STEP
    bc2 = 1.0 - ADAM_B2 ** ADAM_STEP

    def body(g_ref, w_ref, m_ref, v_ref, go_ref, d_ref, mo_ref, vo_ref):
        grad = g_ref[0].astype(F32)
        for i in range(1, n):
            grad = grad + g_ref[i].astype(F32)
        go_ref[...] = grad
        m_new = ADAM_B1 * m_ref[...] + (1.0 - ADAM_B1) * grad
        v_new = ADAM_B2 * v_ref[...] + (1.0 - ADAM_B2) * (grad * grad)
        mo_ref[...] = m_new
        vo_ref[...] = v_new
        d_ref[...] = -ADAM_LR * ((m_new / bc1) / (jnp.sqrt(v_new / bc2) + ADAM_EPS) + ADAM_WD * w_ref[...])

    blk = pl.BlockSpec((tr, c), lambda i: (i, 0))
    out = jax.ShapeDtypeStruct((r, c), F32)
    return pl.pallas_call(
        body, name=name, grid=(r // tr,),
        in_specs=[pl.BlockSpec((n, tr, c), lambda i: (0, i, 0)), blk, blk, blk],
        out_specs=[blk] * 4, out_shape=[out] * 4,
        compiler_params=_cp("parallel"),
    )(g, w, m, v)


def kernel(x, c, ctx, c_ctx, w_mod, b_mod, norm_pre1, norm_post1, norm_pre2, norm_post2, w_in, hg_lb, hg_onorm, gla_w_gk, gla_b_gk, gla_onorm, w_br_hg, w_br_gla, w_out, w_ff_gate, w_ff_up, w_ff_down, loss_target, m_c_ctx, m_w_mod, m_b_mod, m_norm_pre1, m_norm_post1, m_norm_pre2, m_norm_post2, m_w_in, m_hg_lb, m_hg_onorm, m_gla_w_gk, m_gla_b_gk, m_gla_onorm, m_w_br_hg, m_w_br_gla, m_w_out, m_w_ff_gate, m_w_ff_up, m_w_ff_down, v_c_ctx, v_w_mod, v_b_mod, v_norm_pre1, v_norm_post1, v_norm_pre2, v_norm_post2, v_w_in, v_hg_lb, v_hg_onorm, v_gla_w_gk, v_gla_b_gk, v_gla_onorm, v_w_br_hg, v_w_br_gla, v_w_out, v_w_ff_gate, v_w_ff_up, v_w_ff_down):
    xi, yi, ci = lax.axis_index("x"), lax.axis_index("y"), lax.axis_index("c")
    me = 4 * xi + 2 * yi + ci
    t = CTX + x.shape[1]

    c_all, lb_g, wgk_g, bgk_g = _all_gather([c, hg_lb, gla_w_gk[0], gla_b_gk[0]], "ag_small")
    tr_ = lambda a: jnp.swapaxes(a[0], 0, 1)
    big = [w_in[0], w_br_hg[0], w_br_gla[0], w_out[0], tr_(w_ff_gate), tr_(w_ff_up), w_ff_down[0]]
    big_bf = [w.astype(BF16) for w in big]
    cols = lambda g: jnp.transpose(g, (1, 0, 2)).reshape(g.shape[1], N_DEV * g.shape[2])

    def get_w_in(after):
        land, = _split_wait(w_in_handle, "ag_w_in_wait", after)
        return _layout_w_in(cols(_forward_to_sibling(land, "ag_w_in_forward")))

    def get_mix(after):
        g_brh, g_brg, g_out = _split_wait(mix_handle, "ag_mix_wait", after)
        return _gate_cols(cols(g_brh)), _gate_cols(cols(g_brg)), _gate_rows(g_out.reshape(D, D))

    def get_ffn(after):
        g_gate, g_up, g_down = _split_wait(ffn_handle, "ag_ffn_wait", after)
        return (g_gate.reshape(D_FF, D), g_up.reshape(D_FF, D)), g_down.reshape(D_FF, D)

    hg_lb_full = jnp.transpose(lb_g, (1, 2, 0, 3)).reshape(2, 2, HW)
    wgk_k = _layout_wgk(jnp.transpose(wgk_g, (1, 2, 0, 3)).reshape(2, 16, HW)).astype(BF16)
    bgk_k = jnp.transpose(bgk_g, (1, 0, 2)).reshape(1, D)
    onw = jnp.concatenate([jnp.tile(hg_onorm, (1, NH // 2)), jnp.tile(gla_onorm, (1, NH // 2))], axis=1)

    n_mod = w_mod.shape[2]
    a9 = jnp.concatenate([c_ctx[None], c_all[:, 0], jnp.zeros((16 - 1 - N_DEV, D), F32)], axis=0)
    b_loc = lax.dynamic_slice(b_mod, (0, me * n_mod), (1, n_mod))
    s_loc = _mod_fwd(a9, w_mod[0], b_loc)
    s_all, = _all_gather([s_loc], "ag_mod")
    mod_all = jnp.transpose(s_all, (1, 0, 2)).reshape(16, N_DEV * n_mod)
    pad8 = lambda m: jnp.concatenate([m.reshape(6, D), jnp.zeros((2, D), F32)], axis=0)
    modc = pad8(mod_all[0])
    modx = pad8(lax.dynamic_slice(mod_all, (1 + me, 0), (1, N_DEV * n_mod))[0])

    gathered = lambda arrs: [(N_DEV,) + a.shape for a in arrs]
    w_in_handle, tok = _split_start(_view_near, gathered(big_bf[:1]), big_bf[:1], "ag_w_in_start", s_all)
    mix_handle, tok = _split_start(_view_whole, gathered(big_bf[1:4]), big_bf[1:4], "ag_mix_start", tok)
    ffn_handle, tok = _split_start(_view_whole, gathered(big_bf[4:]), big_bf[4:], "ag_ffn_start", tok)

    z = _tie(jnp.concatenate([ctx[0], x[0]], axis=0), tok, "tie_z")
    norms = (norm_pre1, norm_post1, norm_pre2, norm_post2)
    shard = lambda d: jnp.transpose(d.reshape(d.shape[0], N_DEV, -1), (1, 0, 2)).astype(BF16)
    rowshard = lambda d: d.reshape(N_DEV, d.shape[0] // N_DEV, d.shape[1]).astype(BF16)
    sent, w_in_grad = [], {}

    def send_w_in(i, x_after):
        half, rows = W_IN_GRAD_CHUNKS[i]
        handle, tok = _split_start(_view_window(rows), [(N_DEV, rows[1] - rows[0], D)], w_in_grad[half],
                                   "grads_w_in%d_start" % i, x_after)
        w_in_grad[half] = handle["srcs"]
        sent.append(("w_in%d" % i, ["w_in#%d" % i], handle))
        return _tie(x_after, tok, "tie_w_in%d" % i)

    def send(names, grads, x_after):
        if names == ("w_in_a",):
            w_in_grad["a"] = list(grads)
            return send_w_in(0, x_after)
        if names == ("w_in_b",):
            w_in_grad["b"] = list(grads)
            return x_after
        arrs, leaves = [], []
        for nm, g in zip(names, grads):
            if nm in ("w_gate_t", "w_up_t"):
                arrs.append(rowshard(g))
                leaves.append({"w_gate_t": "w_ff_gate", "w_up_t": "w_ff_up"}[nm])
            elif nm == "w_down":
                arrs.append(rowshard(g))
                leaves.append("w_ff_down")
            elif nm == "w_out":
                arrs.append(rowshard(g[GOFF:GOFF + D]))
                leaves.append(nm)
            else:
                arrs.append(shard(g[:, GOFF:GOFF + D]))
                leaves.append(nm)
        handle, tok = _split_start(_view_block, [a.shape for a in arrs], arrs, "grads_%s_start" % names[0], x_after)
        sent.append((names[0], leaves, handle))
        return _tie(x_after, tok, "tie_" + names[0])

    r = _local_step(z, loss_target[0], modc, modx, norms, onw, hg_lb_full, wgk_k, bgk_k,
                    get_w_in, get_mix, get_ffn, send)
    grad_x = r["grad_x"][None]

    sm_pre, sm_mid, sm_fin = r["sm_pre"], r["sm_mid"], r["sm_final"]
    dmodc = jnp.stack([sm_pre[0], sm_pre[2], sm_mid[4], sm_mid[0], sm_mid[2], sm_fin[0]]).reshape(-1)
    dmodx = jnp.stack([sm_pre[1], sm_pre[3], sm_mid[5], sm_mid[1], sm_mid[3], sm_fin[1]]).reshape(-1)
    on = r["sm_post"][0].reshape(NH, DH)
    pieces = [dmodc, dmodx, sm_pre[4], sm_mid[7], sm_mid[6], sm_fin[2], on[:NH // 2].sum(0), on[NH // 2:].sum(0),
              r["d_lb"][:2].reshape(-1), _unlayout_wgk(r["d_wgk"]).reshape(-1), r["d_bgk"][0]]
    loss_local = (0.5 / D) * jnp.sum(r["loss_vec"])
    pieces.append(jnp.concatenate([loss_local.reshape(1), jnp.zeros((DH - 1,), F32)]))
    sizes = [p.shape[0] for p in pieces]
    pack = jnp.concatenate(pieces).reshape(-1, DH)
    pack_all, = _all_gather([pack], "ag_small_grads")
    pack_all = send_w_in(1, pack_all)
    tot = _sum_devices(pack_all).reshape(-1)
    offs = [sum(sizes[:i]) for i in range(len(sizes))]
    part = lambda i: tot[offs[i]:offs[i] + sizes[i]]
    dmodc_t, dmodx_t = part(0), part(1)
    g_b_mod = (dmodc_t + dmodx_t)[None]
    g_norms = [part(i)[None] for i in (2, 3, 4, 5)]
    g_hg_on, g_gla_on = part(6)[None], part(7)[None]
    lb0 = lax.dynamic_slice(part(8).reshape(2, HW), (0, me * (HW // N_DEV)), (2, HW // N_DEV))
    g_hg_lb = jnp.stack([lb0, -lb0])
    g_wgk = lax.dynamic_slice(part(9).reshape(2, 16, HW), (0, 0, me * (HW // N_DEV)), (2, 16, HW // N_DEV))[None]
    g_bgk = lax.dynamic_slice(part(10).reshape(2, HW), (0, me * (HW // N_DEV)), (2, HW // N_DEV))[None]
    loss = part(11)[0]

    dmx_all = pack_all.reshape(N_DEV, -1)[:, sizes[0]:sizes[0] + sizes[1]]
    d9 = jnp.concatenate([lax.dynamic_slice(dmodc_t[None], (0, me * n_mod), (1, n_mod)),
                          lax.dynamic_slice(dmx_all, (0, me * n_mod), (N_DEV, n_mod)),
                          jnp.zeros((16 - 1 - N_DEV, n_mod), F32)], axis=0)
    g_w_mod, dcc_part = _mod_bwd(a9, d9, w_mod[0])
    dcc_all, = _all_gather([dcc_part], "ag_c_ctx")
    dcc_all = send_w_in(2, dcc_all)
    g_c_ctx = _sum_devices(dcc_all)[0]

    recv = {}
    for first, leaves, handle in sent:
        if not first.startswith("w_in"):
            recv.update(zip(leaves, _split_wait(handle, "grads_%s_wait" % first, g_c_ctx)))
    moms = [(m_w_in, v_w_in), (m_w_br_hg, v_w_br_hg), (m_w_br_gla, v_w_br_gla), (m_w_out, v_w_out),
            (m_w_ff_gate, v_w_ff_gate), (m_w_ff_up, v_w_ff_up), (m_w_ff_down, v_w_ff_down)]
    names = ["w_in", "w_br_hg", "w_br_gla", "w_out", "w_ff_gate", "w_ff_up", "w_ff_down"]
    res = {}

    def update(nm, w, m, v):
        if nm in ("w_ff_gate", "w_ff_up"):
            outs = _adamw(recv[nm], w, tr_(m), tr_(v), "adamw_" + nm)
            res[nm] = [jnp.swapaxes(o, 0, 1)[None] for o in outs]
        else:
            res[nm] = [o[None] for o in _adamw(recv[nm], w, m[0], v[0], "adamw_" + nm)]

    for nm, w, (m, v) in list(zip(names, big, moms))[1:]:
        update(nm, w, m, v)
    res["w_mod"] = [o[None] for o in _adamw(g_w_mod[None], w_mod[0], m_w_mod[0], v_w_mod[0], "adamw_w_mod")]

    small = [("c_ctx", c_ctx, m_c_ctx, v_c_ctx, g_c_ctx), ("b_mod", b_mod, m_b_mod, v_b_mod, g_b_mod),
             ("norm_pre1", norm_pre1, m_norm_pre1, v_norm_pre1, g_norms[0]),
             ("norm_post1", norm_post1, m_norm_post1, v_norm_post1, g_norms[1]),
             ("norm_pre2", norm_pre2, m_norm_pre2, v_norm_pre2, g_norms[2]),
             ("norm_post2", norm_post2, m_norm_post2, v_norm_post2, g_norms[3]),
             ("hg_lb", hg_lb, m_hg_lb, v_hg_lb, g_hg_lb), ("hg_onorm", hg_onorm, m_hg_onorm, v_hg_onorm, g_hg_on),
             ("gla_w_gk", gla_w_gk, m_gla_w_gk, v_gla_w_gk, g_wgk), ("gla_b_gk", gla_b_gk, m_gla_b_gk, v_gla_b_gk, g_bgk),
             ("gla_onorm", gla_onorm, m_gla_onorm, v_gla_onorm, g_gla_on)]
    flat = lambda k: jnp.concatenate([s[k].reshape(-1) for s in small]).reshape(-1, DH)
    outs = _adamw(flat(4)[None], flat(1), flat(2), flat(3), "adamw_small")
    off = 0
    for nm, w, _, _, _ in small:
        res[nm] = [o.reshape(-1)[off:off + w.size].reshape(w.shape) for o in outs]
        off += w.size

    done = [res[nm][0] for nm in names[1:]] + [res["w_mod"][0], outs[0]]
    sums = []
    for i, (first, leaves, handle) in enumerate(s for s in sent if s[0].startswith("w_in")):
        half = W_IN_GRAD_CHUNKS[i][0]
        land, = _split_wait(handle, "grads_%s_wait" % first, done, srcs=w_in_grad[half])
        w_in_grad[half] = handle["srcs"]
        sums.append(_sum_windows(land, "sum_windows%d" % i))
    g_t = jnp.concatenate(sums, axis=1)[:W_IN_SHARD]
    lin = lambda a: a.reshape(W_IN_SHARD * D // DH, DH)
    outs = _adamw(lin(g_t)[None], lin(tr_(w_in)), lin(tr_(m_w_in)), lin(tr_(v_w_in)), "adamw_w_in")
    res["w_in"] = [jnp.swapaxes(o.reshape(W_IN_SHARD, D), 0, 1)[None] for o in outs]

    order = ["c_ctx", "w_mod", "b_mod", "norm_pre1", "norm_post1", "norm_pre2", "norm_post2", "w_in", "hg_lb",
             "hg_onorm", "gla_w_gk", "gla_b_gk", "gla_onorm", "w_br_hg", "w_br_gla", "w_out", "w_ff_gate", "w_ff_up",
             "w_ff_down"]
    return (loss, grad_x, *[res[n][k] for k in range(4) for n in order])
```

```python
import functools

import jax
import jax.numpy as jnp
from jax import lax
from jax.experimental import pallas as pl
from jax.experimental.pallas import tpu as pltpu

F32 = jnp.float32
BF16 = jnp.bfloat16
HI = lax.Precision.HIGHEST

N_DEV = 8
D = 1024
CTX = 256
HW = 512
DH = 128
NH = 8
D_FF = 2816
EPS = 1e-6
GLA_NORM = 16.0
CHUNK = 64
TR = 256
NCT = CTX // TR
W_IN_COLS = 7168
MAIN0 = 0
LR0 = 4608
GW = 1152
GOFF = 32
GATE_HG0 = LR0
GATE_GLA0 = LR0 + D
LEVELS = (32, 16, 8)
EXP_CLAMP = 80.0
VMEM_LIMIT = 48 * 1024 * 1024

ADAM_LR, ADAM_B1, ADAM_B2, ADAM_EPS, ADAM_WD, ADAM_STEP = 0.001, 0.9, 0.999, 1e-08, 0.01, 10


def _cp(*sem):
    return pltpu.CompilerParams(dimension_semantics=sem, vmem_limit_bytes=VMEM_LIMIT)


def _sig(x):
    return jax.nn.sigmoid(x)


def _silu(x):
    return x * _sig(x)


def _dsilu(x):
    s = _sig(x)
    return s * (1.0 + x * (1.0 - s))


def _rstd(x):
    return lax.rsqrt(jnp.mean(x * x, axis=-1, keepdims=True) + EPS)


def _rms_bwd(a, y, r):
    return r * (a - y * (r * r) * jnp.mean(a * y, axis=-1, keepdims=True))


def _colsum(x):
    return jnp.sum(x, axis=0, keepdims=True)


def _dot(a, b, dims, precision=None):
    return lax.dot_general(a, b, (dims, ((), ())), preferred_element_type=F32, precision=precision)


NN = ((1,), (0,))
NT = ((1,), (1,))
TN = ((0,), (0,))

SCAN_HEADS_FWD = 4
SCAN_HEADS_BWD = 4


def _split_dot(m, x):
    mb = m.astype(BF16)
    x1 = x.astype(BF16)
    r1 = x - x1.astype(F32)
    x2 = r1.astype(BF16)
    x3 = (r1 - x2.astype(F32)).astype(BF16)
    return _dot(mb, x1, NN) + _dot(mb, x2, NN) + _dot(mb, x3, NN)


def _matmul(a, b, dims, out_dtype, name, tm, tn, tk, a_off=0, m_out=None):
    pair = isinstance(b, (tuple, list))
    bs = list(b) if pair else [b]
    b1 = bs[0]
    rows = b1.shape[0] * len(bs)
    half = None
    if dims == NN:
        m, k, n = a.shape[0], rows, b1.shape[1]
        a_spec = pl.BlockSpec((tm, tk), lambda i, j, kk: (i, kk + a_off))
        half = b1.shape[0] // tk
        b_maps = [lambda i, j, kk: (kk, j)] if not pair else [
            lambda i, j, kk: (jnp.minimum(kk, half - 1), j), lambda i, j, kk: (jnp.maximum(kk - half, 0), j)]
        b_specs = [pl.BlockSpec((tk, tn), f) for f in b_maps]
        axis = 2
    elif dims == NT:
        m, k, n = a.shape[0], b1.shape[1], rows
        a_spec = pl.BlockSpec((tm, tk), lambda i, j, kk: (i, kk + a_off))
        half = b1.shape[0] // tn
        b_maps = [lambda i, j, kk: (j, kk)] if not pair else [
            lambda i, j, kk: (jnp.minimum(j, half - 1), kk), lambda i, j, kk: (jnp.maximum(j - half, 0), kk)]
        b_specs = [pl.BlockSpec((tn, tk), f) for f in b_maps]
        axis = 1
    else:
        assert not pair
        m, k = (a.shape[1] if m_out is None else m_out), a.shape[0]
        n = b1.shape[1]
        a_spec = pl.BlockSpec((tk, tm), lambda i, j, kk: (kk, i + a_off))
        b_specs = [pl.BlockSpec((tk, tn), lambda i, j, kk: (kk, j))]
    assert m % tm == 0 and n % tn == 0 and k % tk == 0, (name, m, n, k, tm, tn, tk)
    nk = k // tk
    nb = len(bs)

    def body(a_ref, *refs):
        o_ref = refs[nb]
        if pair:
            bv = jnp.where(pl.program_id(axis) < half, refs[0][...], refs[1][...])
        else:
            bv = refs[0][...]
        part = _dot(a_ref[...], bv, dims)
        if nk == 1:
            o_ref[...] = part.astype(o_ref.dtype)
            return
        acc_ref = refs[nb + 1]
        kk = pl.program_id(2)

        @pl.when(kk == 0)
        def _():
            acc_ref[...] = part

        @pl.when(kk > 0)
        def _():
            acc_ref[...] += part

        @pl.when(kk == nk - 1)
        def _():
            o_ref[...] = acc_ref[...].astype(o_ref.dtype)

    return pl.pallas_call(
        body,
        name=name,
        grid=(m // tm, n // tn, nk),
        in_specs=[a_spec] + b_specs,
        out_specs=pl.BlockSpec((tm, tn), lambda i, j, kk: (i, j)),
        out_shape=jax.ShapeDtypeStruct((m, n), out_dtype),
        scratch_shapes=[] if nk == 1 else [pltpu.VMEM((tm, tn), F32)],
        compiler_params=_cp("parallel", "parallel", "arbitrary"),
    )(a, *bs)


def _row(c):
    return pl.BlockSpec((TR, c), lambda i: (i, 0))


def _rowcol(width, cb):
    return pl.BlockSpec((TR, width), lambda i: (i, cb))


def _full(shape):
    return pl.BlockSpec(shape, lambda i: (0,) * len(shape))


def _mod_row(mc_ref, mx_ref, k, is_ctx):
    return jnp.where(is_ctx, mc_ref[k:k + 1, :], mx_ref[k:k + 1, :])


def _acc_row(ref, k, val):
    ref[k:k + 1, :] += val


def _acc_mod(ref, k, is_ctx, val):
    zero = jnp.zeros_like(val)
    ref[k:k + 1, :] += jnp.where(is_ctx, val, zero)
    ref[k + 1:k + 2, :] += jnp.where(is_ctx, zero, val)


def _prenorm(z, nw, modc, modx, i_shift, i_scale, name):
    t = z.shape[0]

    def body(z_ref, nw_ref, mc_ref, mx_ref, h_ref):
        is_ctx = pl.program_id(0) < NCT
        x = z_ref[...]
        n = x * _rstd(x) * nw_ref[...]
        h = n * (1.0 + _mod_row(mc_ref, mx_ref, i_scale, is_ctx)) + _mod_row(mc_ref, mx_ref, i_shift, is_ctx)
        h_ref[...] = h.astype(BF16)

    return pl.pallas_call(
        body, name=name, grid=(t // TR,),
        in_specs=[_row(D), _full((1, D)), _full((8, D)), _full((8, D))],
        out_specs=_row(D),
        out_shape=jax.ShapeDtypeStruct((t, D), BF16),
        compiler_params=_cp("parallel"),
    )(z, nw, modc, modx)


def _hg_lb(lb_ref, d):
    a0 = lb_ref[0, d:d + 1, :]
    a1 = lb_ref[1, d:d + 1, :]
    mx = jnp.maximum(a0, a1)
    e0 = jnp.exp(a0 - mx)
    e1 = jnp.exp(a1 - mx)
    return e0 / (e0 + e1)


def _log_sigmoid(x):
    return jnp.minimum(x, 0.0) - jnp.log(1.0 + jnp.exp(-jnp.abs(x)))


def _gates_fwd(p, hg_lb, wgk, bgk):
    t = p.shape[0]
    seg = lambda j: _rowcol(HW, MAIN0 // HW + j)

    def body(hq_ref, hi_ref, hf_ref, hb_ref, gq_ref, gk_ref, gv_ref, lr_ref, lb_ref, wgk_ref, bgk_ref,
             q_ref, v_ref, kf_ref, kb_ref, gf_ref, gb_ref):
        q_ref[:, :HW] = _silu(hq_ref[...].astype(F32)).astype(BF16)
        q_ref[:, HW:] = (gq_ref[...].astype(F32) * (DH ** -0.5)).astype(BF16)
        v_ref[:, :HW] = hi_ref[...]
        v_ref[:, HW:] = gv_ref[...]
        xg = _dot(lr_ref[...].astype(BF16), wgk_ref[...], NN) + bgk_ref[...]
        for d, (raw_ref, k_ref, g_ref) in enumerate(((hf_ref, kf_ref, gf_ref), (hb_ref, kb_ref, gb_ref))):
            lbd = _hg_lb(lb_ref, d)
            f = lbd + (1.0 - lbd) * _sig(raw_ref[...].astype(F32))
            k_ref[:, :HW] = (1.0 - f).astype(BF16)
            k_ref[:, HW:] = gk_ref[...]
            g_ref[:, :HW] = jnp.log(f)
            g_ref[:, HW:] = _log_sigmoid(xg[:, d * HW:(d + 1) * HW]) * (1.0 / GLA_NORM)

    out = jax.ShapeDtypeStruct((t, D), F32)
    outb = jax.ShapeDtypeStruct((t, D), BF16)
    return pl.pallas_call(
        body, name="gates_fwd", grid=(t // TR,),
        in_specs=[seg(0), seg(1), seg(2), seg(3), seg(5), seg(6), seg(7), _rowcol(DH, LR0 // DH),
                  _full((2, 2, HW)), _full((DH, D)), _full((1, D))],
        out_specs=[_row(D)] * 6,
        out_shape=[outb] * 4 + [out] * 2,
        compiler_params=_cp("parallel"),
    )(p, p, p, p, p, p, p, p, hg_lb, wgk, bgk)


def _post_fwd(o_fw, o_bw, p, onw):
    t = o_fw.shape[0]

    def body(of_ref, ob_ref, g1_ref, g2_ref, w_ref, y_ref):
        for h in range(NH):
            sl = slice(h * DH, (h + 1) * DH)
            o = of_ref[:, sl] + ob_ref[:, sl]
            g_ref = g1_ref if h < NH // 2 else g2_ref
            gs = slice((h % (NH // 2)) * DH, (h % (NH // 2) + 1) * DH)
            n = o * _rstd(o) * w_ref[:, sl]
            y_ref[:, sl] = (n * _silu(g_ref[:, gs].astype(F32))).astype(BF16)

    return pl.pallas_call(
        body, name="post_fwd", grid=(t // TR,),
        in_specs=[_row(D), _row(D), _rowcol(HW, MAIN0 // HW + 4), _rowcol(HW, MAIN0 // HW + 8), _full((1, D))],
        out_specs=_row(D),
        out_shape=jax.ShapeDtypeStruct((t, D), BF16),
        compiler_params=_cp("parallel"),
    )(o_fw, o_bw, p, p, onw)


def _gate_window_specs(col0):
    return [_rowcol(HW, col0 // HW), _rowcol(HW, col0 // HW + 1), _rowcol(DH, (col0 + 2 * HW) // DH)]


def _gate_window(refs):
    return jnp.concatenate([r[...].astype(F32) for r in refs], axis=1)


def _merge_fwd(p, u1, u2):
    t = p.shape[0]

    def body(a0, a1, a2, b0, b1, b2, u1_ref, u2_ref, m_ref):
        f = lambda r: r[...].astype(F32)
        m_ref[...] = (_sig(_gate_window((a0, a1, a2))) * f(u1_ref)
                      + _sig(_gate_window((b0, b1, b2))) * f(u2_ref)).astype(BF16)

    return pl.pallas_call(
        body, name="merge_fwd", grid=(t // TR,),
        in_specs=_gate_window_specs(GATE_HG0) + _gate_window_specs(GATE_GLA0) + [_row(GW), _row(GW)],
        out_specs=_row(GW),
        out_shape=jax.ShapeDtypeStruct((t, GW), BF16),
        compiler_params=_cp("parallel"),
    )(p, p, p, p, p, p, u1, u2)


def _mid_fwd(z, y1, nw_post, nw_pre, modc, modx):
    t = z.shape[0]

    def body(z_ref, y_ref, wpo_ref, wpr_ref, mc_ref, mx_ref, z1_ref, h_ref):
        is_ctx = pl.program_id(0) < NCT
        y = y_ref[...].astype(F32)
        z1 = z_ref[...] + _mod_row(mc_ref, mx_ref, 2, is_ctx) * (y * _rstd(y) * wpo_ref[...])
        z1_ref[...] = z1
        n = z1 * _rstd(z1) * wpr_ref[...]
        h = n * (1.0 + _mod_row(mc_ref, mx_ref, 4, is_ctx)) + _mod_row(mc_ref, mx_ref, 3, is_ctx)
        h_ref[...] = h.astype(BF16)

    return pl.pallas_call(
        body, name="mid_fwd", grid=(t // TR,),
        in_specs=[_row(D), _row(D), _full((1, D)), _full((1, D)), _full((8, D)), _full((8, D))],
        out_specs=[_row(D), _row(D)],
        out_shape=[jax.ShapeDtypeStruct((t, D), F32), jax.ShapeDtypeStruct((t, D), BF16)],
        compiler_params=_cp("parallel"),
    )(z, y1, nw_post, nw_pre, modc, modx)


def _swiglu_fwd(uv):
    t = uv.shape[0]

    def body(u_ref, v_ref, a_ref):
        a_ref[...] = (_silu(u_ref[...].astype(F32)) * v_ref[...].astype(F32)).astype(BF16)

    return pl.pallas_call(
        body, name="swiglu_fwd", grid=(t // TR,),
        in_specs=[_rowcol(D_FF, 0), _rowcol(D_FF, 1)],
        out_specs=_row(D_FF),
        out_shape=jax.ShapeDtypeStruct((t, D_FF), BF16),
        compiler_params=_cp("parallel"),
    )(uv, uv)


def _swiglu_bwd(uv, da):
    t = uv.shape[0]

    def body(u_ref, v_ref, da_ref, d_ref):
        u = u_ref[...].astype(F32)
        d = da_ref[...].astype(F32)
        d_ref[:, :D_FF] = (d * v_ref[...].astype(F32) * _dsilu(u)).astype(BF16)
        d_ref[:, D_FF:] = (d * _silu(u)).astype(BF16)

    return pl.pallas_call(
        body, name="swiglu_bwd", grid=(t // TR,),
        in_specs=[_rowcol(D_FF, 0), _rowcol(D_FF, 1), _row(D_FF)],
        out_specs=_row(2 * D_FF),
        out_shape=jax.ShapeDtypeStruct((t, 2 * D_FF), BF16),
        compiler_params=_cp("parallel"),
    )(uv, uv, da)


def _final(z1, y2, target, nw, modc, modx):
    t = z1.shape[0]

    def body(z1_ref, y_ref, tg_ref, w_ref, mc_ref, mx_ref, dz_ref, dy_ref, loss_ref, sm_ref):
        i = pl.program_id(0)
        is_ctx = i < NCT

        @pl.when(i == 0)
        def _():
            loss_ref[...] = jnp.zeros_like(loss_ref)
            sm_ref[...] = jnp.zeros_like(sm_ref)

        g = _mod_row(mc_ref, mx_ref, 5, is_ctx)
        y = y_ref[...].astype(F32)
        r = _rstd(y)
        w = w_ref[...]
        yr = y * r
        n = yr * w
        e = z1_ref[...] + g * n - tg_ref[...]
        lat = jnp.where(is_ctx, 0.0, 1.0)
        loss_ref[...] += lat * _colsum(e * e)
        dz = e * (lat / D)
        dz_ref[...] = dz
        _acc_mod(sm_ref, 0, is_ctx, _colsum(dz * n))
        dn = dz * g
        _acc_row(sm_ref, 2, _colsum(dn * yr))
        dy_ref[...] = _rms_bwd(dn * w, y, r).astype(BF16)

    return pl.pallas_call(
        body, name="final", grid=(t // TR,),
        in_specs=[_row(D), _row(D), pl.BlockSpec((TR, D), lambda i: (jnp.maximum(i - NCT, 0), 0)),
                  _full((1, D)), _full((8, D)), _full((8, D))],
        out_specs=[_row(D), _row(D), _full((1, D)), _full((8, D))],
        out_shape=[jax.ShapeDtypeStruct((t, D), F32), jax.ShapeDtypeStruct((t, D), BF16),
                   jax.ShapeDtypeStruct((1, D), F32), jax.ShapeDtypeStruct((8, D), F32)],
        compiler_params=_cp("arbitrary"),
    )(z1, y2, target, nw, modc, modx)


def _mid_bwd(dh2, dz, z, z1, y1, nw_post, nw_pre, modc, modx):
    t = z.shape[0]

    def body(dh_ref, dz_ref, z_ref, z1_ref, y_ref, wpo_ref, wpr_ref, mc_ref, mx_ref, dzo_ref, dy_ref, sm_ref):
        i = pl.program_id(0)
        is_ctx = i < NCT

        @pl.when(i == 0)
        def _():
            sm_ref[...] = jnp.zeros_like(sm_ref)

        dh = dh_ref[...].astype(F32)
        z1 = z1_ref[...]
        r = _rstd(z1)
        zr = z1 * r
        wpr = wpr_ref[...]
        n = zr * wpr
        _acc_mod(sm_ref, 0, is_ctx, _colsum(dh))
        _acc_mod(sm_ref, 2, is_ctx, _colsum(dh * n))
        dn = dh * (1.0 + _mod_row(mc_ref, mx_ref, 4, is_ctx))
        _acc_row(sm_ref, 6, _colsum(dn * zr))
        dz1 = dz_ref[...] + _rms_bwd(dn * wpr, z1, r)
        dzo_ref[...] = dz1
        y = y_ref[...].astype(F32)
        r1 = _rstd(y)
        yr = y * r1
        wpo = wpo_ref[...]
        g = _mod_row(mc_ref, mx_ref, 2, is_ctx)
        _acc_mod(sm_ref, 4, is_ctx, _colsum(dz1 * (yr * wpo)))
        dn1 = dz1 * g
        _acc_row(sm_ref, 7, _colsum(dn1 * yr))
        dy_ref[...] = _rms_bwd(dn1 * wpo, y, r1).astype(BF16)

    return pl.pallas_call(
        body, name="mid_bwd", grid=(t // TR,),
        in_specs=[_row(D)] * 5 + [_full((1, D)), _full((1, D)), _full((8, D)), _full((8, D))],
        out_specs=[_row(D), _row(D), _full((8, D))],
        out_shape=[jax.ShapeDtypeStruct((t, D), F32), jax.ShapeDtypeStruct((t, D), BF16),
                   jax.ShapeDtypeStruct((8, D), F32)],
        compiler_params=_cp("arbitrary"),
    )(dh2, dz, z, z1, y1, nw_post, nw_pre, modc, modx)


def _pre_bwd(dh1, dz, z, nw, modc, modx):
    t = z.shape[0]

    def body(dh_ref, dz_ref, z_ref, w_ref, mc_ref, mx_ref, dzo_ref, sm_ref):
        i = pl.program_id(0)
        is_ctx = i < NCT

        @pl.when(i == 0)
        def _():
            sm_ref[...] = jnp.zeros_like(sm_ref)

        dh = dh_ref[...].astype(F32)
        x = z_ref[...]
        r = _rstd(x)
        xr = x * r
        w = w_ref[...]
        _acc_mod(sm_ref, 0, is_ctx, _colsum(dh))
        _acc_mod(sm_ref, 2, is_ctx, _colsum(dh * (xr * w)))
        dn = dh * (1.0 + _mod_row(mc_ref, mx_ref, 1, is_ctx))
        _acc_row(sm_ref, 4, _colsum(dn * xr))
        dzo_ref[...] = dz_ref[...] + _rms_bwd(dn * w, x, r)

    return pl.pallas_call(
        body, name="pre_bwd", grid=(t // TR,),
        in_specs=[_row(D)] * 3 + [_full((1, D)), _full((8, D)), _full((8, D))],
        out_specs=[pl.BlockSpec((TR, D), lambda i: (jnp.maximum(i - NCT, 0), 0)), _full((8, D))],
        out_shape=[jax.ShapeDtypeStruct((t - CTX, D), F32), jax.ShapeDtypeStruct((8, D), F32)],
        compiler_params=_cp("arbitrary"),
    )(dh1, dz, z, nw, modc, modx)


def _merge_bwd(dm, p, u1, u2):
    t = dm.shape[0]

    def body(dm_ref, a0, a1, a2, b0, b1, b2, u1_ref, u2_ref, du1_ref, du2_ref, dg_ref):
        dm_ = dm_ref[...].astype(F32)
        s1 = _sig(_gate_window((a0, a1, a2)))
        s2 = _sig(_gate_window((b0, b1, b2)))
        du1_ref[...] = (dm_ * s1).astype(BF16)
        du2_ref[...] = (dm_ * s2).astype(BF16)
        dg_ref[:, :GW] = (dm_ * u1_ref[...].astype(F32) * s1 * (1.0 - s1)).astype(BF16)
        dg_ref[:, GW:] = (dm_ * u2_ref[...].astype(F32) * s2 * (1.0 - s2)).astype(BF16)

    return pl.pallas_call(
        body, name="merge_bwd", grid=(t // TR,),
        in_specs=[_row(GW)] + _gate_window_specs(GATE_HG0) + _gate_window_specs(GATE_GLA0) + [_row(GW), _row(GW)],
        out_specs=[_row(GW), _row(GW), _row(2 * GW)],
        out_shape=[jax.ShapeDtypeStruct((t, GW), BF16), jax.ShapeDtypeStruct((t, GW), BF16),
                   jax.ShapeDtypeStruct((t, 2 * GW), BF16)],
        compiler_params=_cp("parallel"),
    )(dm, p, p, p, p, p, p, u1, u2)


def _post_bwd(dy_hg, dy_gla, o_fw, o_bw, p, onw):
    t = o_fw.shape[0]

    def body(d1_ref, d2_ref, of_ref, ob_ref, g1_ref, g2_ref, w_ref, do_ref, dg_ref, sm_ref):
        @pl.when(pl.program_id(0) == 0)
        def _():
            sm_ref[...] = jnp.zeros_like(sm_ref)

        for h in range(NH):
            sl = slice(h * DH, (h + 1) * DH)
            gs = slice((h % (NH // 2)) * DH, (h % (NH // 2) + 1) * DH)
            g_ref, d_ref = (g1_ref, d1_ref) if h < NH // 2 else (g2_ref, d2_ref)
            o = of_ref[:, sl] + ob_ref[:, sl]
            r = _rstd(o)
            orr = o * r
            w = w_ref[:, sl]
            gt = g_ref[:, gs].astype(F32)
            dy = d_ref[:, gs].astype(F32)
            dg_ref[:, sl] = (dy * (orr * w) * _dsilu(gt)).astype(BF16)
            dn = dy * _silu(gt)
            sm_ref[0:1, sl] += _colsum(dn * orr)
            do_ref[:, sl] = _rms_bwd(dn * w, o, r)

    return pl.pallas_call(
        body, name="post_bwd", grid=(t // TR,),
        in_specs=[_row(HW), _row(HW), _row(D), _row(D), _rowcol(HW, MAIN0 // HW + 4), _rowcol(HW, MAIN0 // HW + 8),
                  _full((1, D))],
        out_specs=[_row(D), _row(D), _full((8, D))],
        out_shape=[jax.ShapeDtypeStruct((t, D), F32), jax.ShapeDtypeStruct((t, D), BF16),
                   jax.ShapeDtypeStruct((8, D), F32)],
        compiler_params=_cp("arbitrary"),
    )(dy_hg, dy_gla, o_fw, o_bw, p, p, onw)


def _gates_bwd(p, hg_lb, wgk, bgk, dgm, dgo, dq_f, dq_b, dv_f, dv_b, dk_f, dk_b, dg_f, dg_b):
    t = p.shape[0]
    seg = lambda j: _rowcol(HW, MAIN0 // HW + j)

    def body(hq_ref, hf_ref, hb_ref, lr_ref, lb_ref, wgk_ref, bgk_ref, dgm_ref, dgo_ref,
             dqf_ref, dqb_ref, dvf_ref, dvb_ref, dkf_ref, dkb_ref, dgf_ref, dgb_ref,
             dp_ref, dlb_ref, dw_ref, db_ref):
        @pl.when(pl.program_id(0) == 0)
        def _():
            dlb_ref[...] = jnp.zeros_like(dlb_ref)
            dw_ref[...] = jnp.zeros_like(dw_ref)
            db_ref[...] = jnp.zeros_like(db_ref)

        c0 = MAIN0

        def put(j, val):
            dp_ref[:, c0 + j * HW:c0 + (j + 1) * HW] = val.astype(BF16)

        dq = dqf_ref[...].astype(F32) + dqb_ref[...].astype(F32)
        dv = dvf_ref[...].astype(F32) + dvb_ref[...].astype(F32)
        put(0, dq[:, :HW] * _dsilu(hq_ref[...].astype(F32)))
        put(1, dv[:, :HW])
        put(5, dq[:, HW:] * (DH ** -0.5))
        put(7, dv[:, HW:])
        put(6, dkf_ref[:, HW:].astype(F32) + dkb_ref[:, HW:].astype(F32))
        dp_ref[:, c0 + 4 * HW:c0 + 5 * HW] = dgo_ref[:, :HW]
        dp_ref[:, c0 + 8 * HW:c0 + 9 * HW] = dgo_ref[:, HW:]
        lr = lr_ref[...].astype(BF16)
        xg = _dot(lr, wgk_ref[...], NN) + bgk_ref[...]
        dxg = []
        for d, (raw_ref, dk_ref, dg_ref) in enumerate(((hf_ref, dkf_ref, dgf_ref), (hb_ref, dkb_ref, dgb_ref))):
            lbd = _hg_lb(lb_ref, d)
            s = _sig(raw_ref[...].astype(F32))
            f = lbd + (1.0 - lbd) * s
            df = dg_ref[:, :HW] / f - dk_ref[:, :HW].astype(F32)
            put(2 + d, df * (1.0 - lbd) * s * (1.0 - s))
            dlb_ref[d:d + 1, :] += _colsum(df * (1.0 - s)) * (lbd * (1.0 - lbd))
            dxg.append(dg_ref[:, HW:] * (1.0 / GLA_NORM) * _sig(-xg[:, d * HW:(d + 1) * HW]))
        dxg = jnp.concatenate(dxg, axis=1)
        db_ref[0:1, :] += _colsum(dxg)
        dxg_b = dxg.astype(BF16)
        dw_ref[...] += _dot(lr, dxg_b, TN)
        dlr = _dot(dxg_b, wgk_ref[...], NT)
        dp_ref[:, LR0:LR0 + DH] = (dlr + dgm_ref[:, :DH].astype(F32)).astype(BF16)
        dp_ref[:, LR0 + DH:GATE_GLA0] = dgm_ref[:, DH:D]
        dp_ref[:, GATE_GLA0:GATE_GLA0 + DH] = dgm_ref[:, D:GW] + dgm_ref[:, GW:GW + DH]
        dp_ref[:, GATE_GLA0 + DH:GATE_GLA0 + GW] = dgm_ref[:, GW + DH:]
        dp_ref[:, GATE_GLA0 + GW:] = jnp.zeros((TR, W_IN_COLS - GATE_GLA0 - GW), BF16)

    return pl.pallas_call(
        body, name="gates_bwd", grid=(t // TR,),
        in_specs=[seg(0), seg(2), seg(3), _rowcol(DH, LR0 // DH), _full((2, 2, HW)), _full((DH, D)), _full((1, D)),
                  _row(2 * GW), _row(D)] + [_row(D)] * 8,
        out_specs=[_row(W_IN_COLS), _full((8, HW)), _full((DH, D)), _full((8, D))],
        out_shape=[jax.ShapeDtypeStruct((t, W_IN_COLS), BF16), jax.ShapeDtypeStruct((8, HW), F32),
                   jax.ShapeDtypeStruct((DH, D), F32), jax.ShapeDtypeStruct((8, D), F32)],
        compiler_params=_cp("arbitrary"),
    )(p, p, p, p, hg_lb, wgk, bgk, dgm, dgo, dq_f, dq_b, dv_f, dv_b, dk_f, dk_b, dg_f, dg_b)


def _scan_consts(rev):
    r = lax.broadcasted_iota(jnp.int32, (CHUNK, CHUNK), 0)
    u = lax.broadcasted_iota(jnp.int32, (CHUNK, CHUNK), 1)
    rp = lax.broadcasted_iota(jnp.int32, (CHUNK, 1), 0)
    if rev:
        r, u, rp = CHUNK - 1 - r, CHUNK - 1 - u, CHUNK - 1 - rp
    tri = jnp.where(u <= r, 1.0, 0.0).astype(F32)
    tri_t = jnp.where(r <= u, 1.0, 0.0).astype(F32)
    lv = []
    for b in LEVELS:
        sh = b.bit_length() - 1
        pair = ((r >> sh) == (u >> sh) + 1) & (((u >> sh) & 1) == 0)
        pair_t = ((u >> sh) == (r >> sh) + 1) & (((r >> sh) & 1) == 0)
        tside = ((rp >> sh) & 1) == 1
        lv.append((pair, pair_t, tside))
    bd = LEVELS[-1].bit_length() - 1
    diag = ((r >> bd) == (u >> bd)) & (u <= r)
    diag_t = ((r >> bd) == (u >> bd)) & (r <= u)
    return tri, tri_t, lv, diag, diag_t


def _row_of(pos, rev):
    return CHUNK - 1 - pos if rev else pos


def _chunk_terms(cum, b_scr, consts, rev):
    _, _, lv, _, _ = consts
    terms = []
    for b, (_, _, tside) in zip(LEVELS, lv):
        pieces = []
        for j in range(CHUNK // (2 * b)):
            row = _row_of(2 * b * j + b - 1, rev)
            pieces.append(jnp.broadcast_to(b_scr[row:row + 1, :], (2 * b, DH)))
        if rev:
            pieces = pieces[::-1]
        bnd = pieces[0] if len(pieces) == 1 else jnp.concatenate(pieces, axis=0)
        w = jnp.exp(jnp.minimum(jnp.where(tside, cum - bnd, bnd - cum), 0.0))
        wq = jnp.where(tside, w, 0.0)
        wk = jnp.where(tside, 0.0, w)
        terms.append((wq, wk))
    b = LEVELS[-1]
    pieces = []
    for j in range(CHUNK // b):
        if j == 0:
            pieces.append(jnp.zeros((b, DH), F32))
        else:
            row = _row_of(b * j - 1, rev)
            pieces.append(jnp.broadcast_to(b_scr[row:row + 1, :], (b, DH)))
    if rev:
        pieces = pieces[::-1]
    start = jnp.concatenate(pieces, axis=0)
    wq = jnp.exp(jnp.minimum(cum - start, 0.0))
    wk = jnp.exp(jnp.minimum(start - cum, EXP_CLAMP))
    terms.append((wq, wk))
    return terms


def _run_staged(units):
    live = list(units)
    while live:
        nxt = []
        for u in live:
            try:
                next(u)
                nxt.append(u)
            except StopIteration:
                pass
        live = nxt


SCAN_TB = 256
SCAN_CB = SCAN_TB // CHUNK


def _block_order(i, ntb, rev):
    nctx = CTX // SCAN_TB
    if not rev:
        return i
    return jnp.where(i < nctx, nctx - 1 - i, ntb - 1 - (i - nctx))


def _chunk_in_block(j, rev):
    return SCAN_CB - 1 - j if rev else j


def _scan_fwd(q, k, v, g, rev):
    t = q.shape[0]
    nc = t // CHUNK
    hpb = SCAN_HEADS_FWD

    def body(q_ref, k_ref, v_ref, g_ref, o_ref, st_ref, s_scr, b_scr):
        consts = _scan_consts(rev)
        _, _, lv, diag, _ = consts
        masks = [pair for pair, _, _ in lv] + [diag]

        @pl.when(pl.program_id(1) == 0)
        def _():
            s_scr[...] = jnp.zeros_like(s_scr)

        tri = consts[0]
        state = {hh: s_scr[hh] for hh in range(hpb)}

        def unit(hh, j):
            sl = slice(hh * DH, (hh + 1) * DH)
            c = _chunk_in_block(j, rev)
            rows = slice(c * CHUNK, (c + 1) * CHUNK)
            b_ref = b_scr.at[hh * SCAN_CB + j]
            qc, kc, vc, gc = q_ref[rows, sl], k_ref[rows, sl], v_ref[rows, sl], g_ref[rows, sl]
            cum = _split_dot(tri, gc)
            b_ref[...] = cum
            yield
            terms = _chunk_terms(cum, b_ref, consts, rev)
            ops = [((qc * wq).astype(BF16), (kc * wk).astype(BF16)) for wq, wk in terms]
            tot = _colsum(gc)
            qe = (qc * jnp.exp(cum)).astype(BF16)
            ke = (kc * jnp.exp(tot - cum)).astype(BF16)
            vb = vc.astype(BF16)
            yield
            scs = [_dot(qt, kt, NT) for qt, kt in ops]
            kv = _dot(vb, ke, TN)
            yield
            a = jnp.zeros((CHUNK, CHUNK), F32)
            for sc, m in zip(scs, masks):
                a = a + jnp.where(m, sc, 0.0)
            o_intra = _dot(a.astype(BF16), vb, NN)
            yield
            st = state[hh]
            st_ref[hh, c] = st
            o_ref[rows, sl] = o_intra + _dot(qe, st.astype(BF16), NT)
            state[hh] = st * jnp.exp(tot) + kv
            yield

        _run_staged([unit(hh, j) for hh in range(hpb) for j in range(SCAN_CB)])
        for hh in range(hpb):
            s_scr[hh] = state[hh]

    ntb = t // SCAN_TB
    col = pl.BlockSpec((SCAN_TB, hpb * DH), lambda h, i: (_block_order(i, ntb, rev), h))
    return pl.pallas_call(
        body, name="scan_fwd_" + ("bw" if rev else "fw"), grid=(NH // hpb, ntb),
        in_specs=[col] * 4,
        out_specs=[col, pl.BlockSpec((hpb, SCAN_CB, DH, DH), lambda h, i: (h, _block_order(i, ntb, rev), 0, 0))],
        out_shape=[jax.ShapeDtypeStruct((t, D), F32), jax.ShapeDtypeStruct((NH, nc, DH, DH), F32)],
        scratch_shapes=[pltpu.VMEM((hpb, DH, DH), F32), pltpu.VMEM((hpb * SCAN_CB, CHUNK, DH), F32)],
        compiler_params=_cp("parallel", "arbitrary"),
    )(q, k, v, g)


def _scan_bwd(q, k, v, g, do, states, rev):
    t = q.shape[0]
    nc = t // CHUNK
    hpb = SCAN_HEADS_BWD

    def body(q_ref, k_ref, v_ref, g_ref, do_ref, st_ref, dq_ref, dk_ref, dv_ref, dg_ref, ds_scr, b_scr):
        consts = _scan_consts(rev)
        _, tri_t, lv, diag, diag_t = consts
        masks = [(pair, pair_t) for pair, pair_t, _ in lv] + [(diag, diag_t)]
        @pl.when(pl.program_id(1) == 0)
        def _():
            ds_scr[...] = jnp.zeros_like(ds_scr)

        tri = consts[0]
        dstate = {hh: ds_scr[hh] for hh in range(hpb)}

        def unit(hh, jj):
            sl = slice(hh * DH, (hh + 1) * DH)
            c = _chunk_in_block(SCAN_CB - 1 - jj, rev)
            rows = slice(c * CHUNK, (c + 1) * CHUNK)
            b_ref = b_scr.at[hh * SCAN_CB + jj]
            qc, kc, vc, gc = q_ref[rows, sl], k_ref[rows, sl], v_ref[rows, sl], g_ref[rows, sl]
            dob = do_ref[rows, sl].astype(BF16)
            vb = vc.astype(BF16)
            cum = _split_dot(tri, gc)
            b_ref[...] = cum
            da = _dot(dob, vb, NT)
            da_t = _dot(vb, dob, NT)
            yield
            terms = _chunk_terms(cum, b_ref, consts, rev)
            ops = [((qc * wq).astype(BF16), (kc * wk).astype(BF16)) for wq, wk in terms]
            tot = _colsum(gc)
            e_tot = jnp.exp(tot)
            e_b = jnp.exp(cum)
            e_t = jnp.exp(tot - cum)
            qeb = (qc * e_b).astype(BF16)
            keb = (kc * e_t).astype(BF16)
            dal = [(jnp.where(m, da, 0.0).astype(BF16), jnp.where(m_t, da_t, 0.0).astype(BF16)) for m, m_t in masks]
            yield
            ats = [_dot(ktb, qtb, NT) for qtb, ktb in ops]
            dqts = [_dot(d, ktb, NN) for (d, _), (_, ktb) in zip(dal, ops)]
            dkts = [_dot(d_t, qtb, NN) for (_, d_t), (qtb, _) in zip(dal, ops)]
            qd = _dot(dob, qeb, TN)
            yield
            a_t = jnp.zeros((CHUNK, CHUNK), F32)
            dq = jnp.zeros((CHUNK, DH), F32)
            dk = jnp.zeros((CHUNK, DH), F32)
            db = jnp.zeros((CHUNK, DH), F32)
            for at, dqt, dkt, (wq, wk), (qtb, ktb), (_, m_t) in zip(ats, dqts, dkts, terms, ops, masks):
                a_t = a_t + jnp.where(m_t, at, 0.0)
                dq = dq + dqt * wq
                dk = dk + dkt * wk
                db = db + dqt * qtb.astype(F32) - dkt * ktb.astype(F32)
            dv_intra = _dot(a_t.astype(BF16), dob, NN)
            st = st_ref[hh, c]
            stb = st.astype(BF16)
            dqe = _dot(dob, stb, NN)
            yield
            dst = dstate[hh]
            dstb = dst.astype(BF16)
            dstate[hh] = dst * e_tot + qd
            dv_ref[rows, sl] = (dv_intra + _dot(keb, dstb, NT)).astype(BF16)
            dke = _dot(vb, dstb, NN)
            yield
            qe = qeb.astype(F32)
            ke = keb.astype(F32)
            dq_ref[rows, sl] = (dq + dqe * e_b).astype(BF16)
            dk_ref[rows, sl] = (dk + dke * e_t).astype(BF16)
            db = db + dqe * qe - dke * ke
            dtot = _colsum(dstb.astype(F32) * stb.astype(F32)) * e_tot + _colsum(dke * ke)
            dg_ref[rows, sl] = _split_dot(tri_t, db) + dtot
            yield

        _run_staged([unit(hh, jj) for hh in range(hpb) for jj in range(SCAN_CB)])
        for hh in range(hpb):
            ds_scr[hh] = dstate[hh]

    ntb = t // SCAN_TB
    blk = lambda i: _block_order(ntb - 1 - i, ntb, rev)
    col = pl.BlockSpec((SCAN_TB, hpb * DH), lambda h, i: (blk(i), h))
    out = jax.ShapeDtypeStruct((t, D), F32)
    outb = jax.ShapeDtypeStruct((t, D), BF16)
    return pl.pallas_call(
        body, name="scan_bwd_" + ("bw" if rev else "fw"), grid=(NH // hpb, ntb),
        in_specs=[col] * 5 + [pl.BlockSpec((hpb, SCAN_CB, DH, DH), lambda h, i: (h, blk(i), 0, 0))],
        out_specs=[col] * 4,
        out_shape=[outb] * 3 + [out],
        scratch_shapes=[pltpu.VMEM((hpb, DH, DH), F32), pltpu.VMEM((hpb * SCAN_CB, CHUNK, DH), F32)],
        compiler_params=_cp("parallel", "arbitrary"),
    )(q, k, v, g, do, states)


W_IN_GRAD_CHUNKS = (("a", (0, 512)), ("b", (0, 128)), ("b", (128, 512)))
W_IN_REF = 6688


def _layout_w_in(w):
    return jnp.pad(w, ((0, 0), (0, W_IN_COLS - W_IN_REF)))


def _unlayout_w_in(d):
    return d[:, :W_IN_REF]


def _gate_cols(w):
    return jnp.pad(w, ((0, 0), (GOFF, GW - GOFF - D)))


def _gate_rows(w):
    return jnp.pad(w, ((GOFF, GW - GOFF - D), (0, 0)))


def _layout_wgk(w):
    r = w.shape[1]
    top = jnp.concatenate([w[0], jnp.zeros_like(w[0])], axis=1)
    bot = jnp.concatenate([jnp.zeros_like(w[1]), w[1]], axis=1)
    return jnp.concatenate([top, bot, jnp.zeros((DH - 2 * r, D), w.dtype)], axis=0)


def _unlayout_wgk(d, r=16):
    return jnp.stack([d[:r, :HW], d[r:2 * r, HW:]])


def _local_step(z, target, modc, modx, norms, onw, hg_lb, wgk, bgk, get_w_in, get_mix, get_ffn, send):
    n_pre1, n_post1, n_pre2, n_post2 = norms
    t = z.shape[0]
    tm = 1152 if t % 1152 == 0 else 256
    h1 = _prenorm(z, n_pre1, modc, modx, 0, 1, "prenorm1")
    w_in = get_w_in(h1)
    p = _matmul(h1, w_in, NN, BF16, "mm_in", tm, 1024, D)
    q, v, k_f, k_b, g_f, g_b = _gates_fwd(p, hg_lb, wgk, bgk)
    o_f, st_f = _scan_fwd(q, k_f, v, g_f, False)
    o_b, st_b = _scan_fwd(q, k_b, v, g_b, True)
    y = _post_fwd(o_f, o_b, p, onw)
    w_br_hg, w_br_gla, w_out = get_mix(y)
    u1 = _matmul(y, w_br_hg, NN, BF16, "mm_br_hg", tm, GW, HW, a_off=0)
    u2 = _matmul(y, w_br_gla, NN, BF16, "mm_br_gla", tm, GW, HW, a_off=1)
    merged = _merge_fwd(p, u1, u2)
    y1 = _matmul(merged, w_out, NN, BF16, "mm_out", tm, 512, GW)
    z1, h2 = _mid_fwd(z, y1, n_post1, n_pre2, modc, modx)
    w_gu_t, w_down = get_ffn(h2)
    uv = _matmul(h2, w_gu_t, NT, BF16, "mm_gu", tm, D_FF // 2, D)
    act = _swiglu_fwd(uv)
    y2 = _matmul(act, w_down, NN, BF16, "mm_down", tm, 512, D_FF)
    dz, dy2, loss_vec, sm_final = _final(z1, y2, target, n_post2, modc, modx)
    dact = _matmul(dy2, w_down, NT, BF16, "mm_down_dx", tm, D_FF // 2, D)
    d_w_down = _matmul(act, dy2, TN, BF16, "mm_down_dw", D_FF // 2, 512, t)
    duv = _swiglu_bwd(uv, dact)
    dh2 = _matmul(duv, w_gu_t, NN, BF16, "mm_gu_dx", tm, 512, D_FF)
    d_w_gate_t = _matmul(duv, h2, TN, BF16, "mm_gate_dw", D_FF // 2, 512, t, a_off=0, m_out=D_FF)
    d_w_up_t = _matmul(duv, h2, TN, BF16, "mm_up_dw", D_FF // 2, 512, t, a_off=2, m_out=D_FF)
    dh2 = send(("w_down", "w_gate_t", "w_up_t"), (d_w_down, d_w_gate_t, d_w_up_t), dh2)
    dz, dy1, sm_mid = _mid_bwd(dh2, dz, z, z1, y1, n_post1, n_pre2, modc, modx)
    dmerged = _matmul(dy1, w_out, NT, BF16, "mm_out_dx", tm, GW, D)
    d_w_out = _matmul(merged, dy1, TN, BF16, "mm_out_dw", GW, 512, t)
    du1, du2, dgm = _merge_bwd(dmerged, p, u1, u2)
    dy_hg = _matmul(du1, w_br_hg, NT, BF16, "mm_br_hg_dx", tm, HW, GW)
    dy_gla = _matmul(du2, w_br_gla, NT, BF16, "mm_br_gla_dx", tm, HW, GW)
    d_w_br_hg = _matmul(y, du1, TN, BF16, "mm_br_hg_dw", HW, GW, t, a_off=0, m_out=HW)
    d_w_br_gla = _matmul(y, du2, TN, BF16, "mm_br_gla_dw", HW, GW, t, a_off=1, m_out=HW)
    dy_hg = send(("w_out", "w_br_hg", "w_br_gla"), (d_w_out, d_w_br_hg, d_w_br_gla), dy_hg)
    do, dgo, sm_post = _post_bwd(dy_hg, dy_gla, o_f, o_b, p, onw)
    dq_f, dk_f, dv_f, dg_f = _scan_bwd(q, k_f, v, g_f, do, st_f, False)
    dq_b, dk_b, dv_b, dg_b = _scan_bwd(q, k_b, v, g_b, do, st_b, True)
    dp, d_lb, d_wgk, d_bgk = _gates_bwd(p, hg_lb, wgk, bgk, dgm, dgo, dq_f, dq_b, dv_f, dv_b, dk_f, dk_b, dg_f, dg_b)
    d_w_in_a = _matmul(h1, dp, TN, BF16, "mm_in_dw_a", 512, 1024, t, a_off=0, m_out=D // 2)
    dp = send(("w_in_a",), (d_w_in_a,), dp)
    d_w_in_b = _matmul(h1, dp, TN, BF16, "mm_in_dw_b", 512, 1024, t, a_off=1, m_out=D // 2)
    dp = send(("w_in_b",), (d_w_in_b,), dp)
    dh1 = _matmul(dp, w_in, NT, BF16, "mm_in_dx", tm, 512, W_IN_COLS // 2)
    grad_x, sm_pre = _pre_bwd(dh1, dz, z, n_pre1, modc, modx)
    return dict(loss_vec=loss_vec, grad_x=grad_x, sm_final=sm_final, sm_mid=sm_mid, sm_post=sm_post, sm_pre=sm_pre,
                d_lb=d_lb, d_wgk=d_wgk, d_bgk=d_bgk)


MESH = pl.DeviceIdType.MESH
ANY = pl.BlockSpec(memory_space=pl.ANY)
N_REL = N_DEV - 1


def _place():
    return lax.axis_index("x"), lax.axis_index("y"), lax.axis_index("c")


def _slot(p):
    return 4 * p[0] + 2 * p[1] + p[2]


def _all_gather(arrays, name):
    n = len(arrays)

    def body(*refs):
        ins, outs = refs[:n], refs[n:2 * n]
        send_sems, recv_sems, local_sems = refs[2 * n:]
        x, y, c = _place()
        me, sibling = (x, y, c), (x, y, 1 - c)
        chips = [(1 - x, y), (x, 1 - y), (1 - x, 1 - y)]

        def copy(a, k, block, to, src=None):
            dst = outs[a].at[_slot(block)]
            return pltpu.make_async_remote_copy(
                src_ref=dst if src is None else src, dst_ref=dst,
                send_sem=send_sems.at[N_REL * a + k], recv_sem=recv_sems.at[N_REL * a + k],
                device_id=to, device_id_type=MESH)

        mine = [pltpu.make_async_copy(ins[a], outs[a].at[_slot(me)], local_sems.at[a]) for a in range(n)]
        for cp in mine:
            cp.start()
        first = []
        for a in range(n):
            first.append(copy(a, 0, me, sibling, src=ins[a]))
            first += [copy(a, 1 + j, me, (*chip, c), src=ins[a]) for j, chip in enumerate(chips)]
        for cp in first:
            cp.start()
        passed = []
        for j, chip in enumerate(chips):
            for a in range(n):
                copy(a, 1 + j, (*chip, c), me).wait_recv()
                fwd = copy(a, 4 + j, (*chip, c), sibling)
                fwd.start()
                passed.append(fwd)
        for a in range(n):
            copy(a, 0, sibling, me).wait_recv()
        for j, chip in enumerate(chips):
            for a in range(n):
                copy(a, 4 + j, (*chip, 1 - c), me).wait_recv()
        for cp in first + passed:
            cp.wait_send()
        for cp in mine:
            cp.wait()

    return pl.pallas_call(
        body, name=name,
        in_specs=[ANY] * n, out_specs=[ANY] * n,
        out_shape=[jax.ShapeDtypeStruct((N_DEV,) + a.shape, a.dtype) for a in arrays],
        scratch_shapes=[pltpu.SemaphoreType.DMA((N_REL * n,)), pltpu.SemaphoreType.DMA((N_REL * n,)),
                        pltpu.SemaphoreType.DMA((n,))],
    )(*arrays)


def _exchange(arrays, name):
    n = len(arrays)

    def body(*refs):
        ins, outs = refs[:n], refs[n:2 * n]
        send_sems, recv_sems, local_sems = refs[2 * n:]
        x, y, c = _place()
        me = _slot((x, y, c))
        mine = [pltpu.make_async_copy(ins[a].at[me], outs[a].at[me], local_sems.at[a]) for a in range(n)]
        for cp in mine:
            cp.start()
        copies = []
        for a in range(n):
            for k in range(1, N_DEV):
                flip = lambda v, bit: 1 - v if bit else v
                peer = (flip(x, k & 4), flip(y, k & 2), flip(c, k & 1))
                copies.append(pltpu.make_async_remote_copy(
                    src_ref=ins[a].at[_slot(peer)], dst_ref=outs[a].at[me],
                    send_sem=send_sems.at[N_REL * a + k - 1], recv_sem=recv_sems.at[N_REL * a + k - 1],
                    device_id=peer, device_id_type=MESH))
                copies[-1].start()
        i = 0
        for a in range(n):
            for k in range(1, N_DEV):
                flip = lambda v, bit: 1 - v if bit else v
                peer = (flip(x, k & 4), flip(y, k & 2), flip(c, k & 1))
                pltpu.make_async_remote_copy(
                    src_ref=ins[a].at[_slot(peer)], dst_ref=outs[a].at[_slot(peer)],
                    send_sem=send_sems.at[N_REL * a + k - 1], recv_sem=recv_sems.at[N_REL * a + k - 1],
                    device_id=peer, device_id_type=MESH).wait_recv()
                i += 1
        for cp in copies:
            cp.wait_send()
        for cp in mine:
            cp.wait()

    return pl.pallas_call(
        body, name=name,
        in_specs=[ANY] * n, out_specs=[ANY] * n,
        out_shape=[jax.ShapeDtypeStruct(a.shape, a.dtype) for a in arrays],
        scratch_shapes=[pltpu.SemaphoreType.DMA((N_REL * n,)), pltpu.SemaphoreType.DMA((N_REL * n,)),
                        pltpu.SemaphoreType.DMA((n,))],
    )(*arrays)


HBM = pl.BlockSpec(memory_space=pltpu.HBM)
SEM = pl.BlockSpec(memory_space=pltpu.SEMAPHORE)
EFFECT = pltpu.SideEffectType.DATAFLOW_SIDE_EFFECTING


def _peer_of(x, y, c, k):
    flip = lambda v, bit: 1 - v if bit else v
    return flip(x, k & 4), flip(y, k & 2), flip(c, k & 1)


def _view_whole(src, slot):
    return src


def _view_near(src, slot):
    return src


_view_near.peers = (1, 2, 4, 6)


def _view_block(src, slot):
    return src.at[slot]


W_IN_SHARD = W_IN_REF // N_DEV


def _view_window(rows):
    def view(src, slot):
        col0 = pl.multiple_of((W_IN_SHARD * slot // DH) * DH, DH)
        return src.at[pl.ds(rows[0], rows[1] - rows[0]), pl.ds(col0, D)]
    return view


def _split_copies(view, srcs, lands, send_sems, recv_sems, local_sems):
    x, y, c = _place()
    me = _slot((x, y, c))
    local, sends, waits = [], [], []
    for a, (src, land) in enumerate(zip(srcs, lands)):
        local.append(pltpu.make_async_copy(view(src, me), land.at[me], local_sems.at[a]))
        for k in getattr(view, "peers", range(1, N_DEV)):
            peer = _peer_of(x, y, c, k)
            mine = view(src, _slot(peer))
            sems = dict(send_sem=send_sems.at[N_REL * a + k - 1], recv_sem=recv_sems.at[N_REL * a + k - 1],
                        device_id=peer, device_id_type=MESH)
            sends.append(pltpu.make_async_remote_copy(src_ref=mine, dst_ref=land.at[me], **sems))
            waits.append(pltpu.make_async_remote_copy(src_ref=mine, dst_ref=land.at[_slot(peer)], **sems))
    return local, sends, waits


def _split_start(view, land_shapes, srcs, name, after):
    n = len(srcs)
    lands = [lax.empty(shp, s.dtype) for shp, s in zip(land_shapes, srcs)]

    def body(*refs):
        src_refs, land_refs = refs[:n], refs[n:2 * n]
        send_sems, recv_sems, local_sems = refs[2 * n + 1:2 * n + 4]
        token = refs[-1]
        local, sends, _ = _split_copies(view, src_refs, land_refs, send_sems, recv_sems, local_sems)
        for cp in local + sends:
            cp.start()
        token[...] = jnp.zeros_like(token)

    hbm = lambda a: pltpu.with_memory_space_constraint(a, pltpu.HBM)
    out = pl.pallas_call(
        body, name=name,
        out_shape=(pltpu.SemaphoreType.DMA((N_REL * n,)), pltpu.SemaphoreType.DMA((N_REL * n,)),
                   pltpu.SemaphoreType.DMA((n,)),
                   *[pltpu.HBM(s.shape, s.dtype) for s in srcs], *[pltpu.HBM(l.shape, l.dtype) for l in lands],
                   jax.ShapeDtypeStruct((8, DH), F32)),
        in_specs=[HBM] * (2 * n) + [ANY],
        out_specs=(SEM, SEM, SEM, *([HBM] * (2 * n)), pl.BlockSpec(memory_space=pltpu.VMEM)),
        input_output_aliases={i: 3 + i for i in range(2 * n)},
        compiler_params=pltpu.CompilerParams(has_side_effects=EFFECT),
    )(*[hbm(s) for s in srcs], *[hbm(l) for l in lands], after)
    handle = dict(view=view, n=n, sems=out[:3], srcs=list(out[3:3 + n]), lands=list(out[3 + n:3 + 2 * n]))
    return handle, out[-1]


def _split_wait(handle, name, after, srcs=None):
    view, n, sems, lands = handle["view"], handle["n"], handle["sems"], handle["lands"]
    srcs = handle["srcs"] if srcs is None else srcs
    afters = list(after) if isinstance(after, (list, tuple)) else [after]

    def body(*refs):
        src_refs, land_refs = refs[:n], refs[n:2 * n]
        send_sems, recv_sems, local_sems = refs[2 * n:2 * n + 3]
        local, _, waits = _split_copies(view, src_refs, land_refs, send_sems, recv_sems, local_sems)
        for cp in waits:
            cp.wait_send()
            cp.wait_recv()
        for cp in local:
            cp.wait()

    out = pl.pallas_call(
        body, name=name,
        out_shape=(*[pltpu.HBM(s.shape, s.dtype) for s in srcs], *[pltpu.HBM(l.shape, l.dtype) for l in lands]),
        in_specs=[HBM] * (2 * n) + [SEM, SEM, SEM] + [ANY] * len(afters),
        out_specs=tuple([HBM] * (2 * n)),
        input_output_aliases={i: i for i in range(2 * n)},
        compiler_params=pltpu.CompilerParams(has_side_effects=EFFECT),
    )(*srcs, *lands, *sems, *afters)
    handle["srcs"] = list(out[:n])
    return list(out[n:])


def _tie(x, token, name):
    def body(x_ref, t_ref, o_ref):
        pass

    return pl.pallas_call(
        body, name=name, out_shape=jax.ShapeDtypeStruct(x.shape, x.dtype),
        in_specs=[ANY, ANY], out_specs=ANY, input_output_aliases={0: 0},
    )(x, token)


def _forward_to_sibling(land, name):
    def body(land_ref, out_ref, send_sems, recv_sems):
        x, y, c = _place()
        sibling = (x, y, 1 - c)
        chips = [(1 - x, y), (x, 1 - y), (1 - x, 1 - y)]

        def copy(j, core):
            blk = _slot((*chips[j], core))
            return pltpu.make_async_remote_copy(src_ref=land_ref.at[blk], dst_ref=out_ref.at[blk],
                                                send_sem=send_sems.at[j], recv_sem=recv_sems.at[j],
                                                device_id=sibling, device_id_type=MESH)

        sends = [copy(j, c) for j in range(3)]
        for cp in sends:
            cp.start()
        for j in range(3):
            copy(j, 1 - c).wait_recv()
        for cp in sends:
            cp.wait_send()

    return pl.pallas_call(
        body, name=name, in_specs=[ANY], out_specs=ANY, input_output_aliases={0: 0},
        out_shape=jax.ShapeDtypeStruct(land.shape, land.dtype),
        scratch_shapes=[pltpu.SemaphoreType.DMA((3,)), pltpu.SemaphoreType.DMA((3,))],
    )(land)


def _mod_fwd(a, w, b):
    def body(a_ref, w_ref, b_ref, o_ref):
        o_ref[...] = _dot(_silu(a_ref[...]), w_ref[...], NN, precision=HI) + b_ref[...]

    return pl.pallas_call(
        body, name="mod_fwd", out_shape=jax.ShapeDtypeStruct((a.shape[0], w.shape[1]), F32),
        compiler_params=pltpu.CompilerParams(vmem_limit_bytes=VMEM_LIMIT),
    )(a, w, b)


def _mod_bwd(a, d, w):
    def body(a_ref, d_ref, w_ref, dw_ref, dc_ref):
        av = a_ref[...]
        dv = d_ref[...]
        dw_ref[...] = _dot(_silu(av), dv, TN, precision=HI)
        da = _dot(dv[0:8, :], w_ref[...], NT, precision=HI) * _dsilu(av[0:8, :])
        row = lax.broadcasted_iota(jnp.int32, da.shape, 0)
        dc_ref[...] = jnp.where(row == 0, da, 0.0)

    return pl.pallas_call(
        body, name="mod_bwd",
        out_shape=[jax.ShapeDtypeStruct(w.shape, F32), jax.ShapeDtypeStruct((8, w.shape[0]), F32)],
        compiler_params=pltpu.CompilerParams(vmem_limit_bytes=VMEM_LIMIT),
    )(a, d, w)


def _sum_devices(g):
    def body(g_ref, o_ref):
        acc = g_ref[0]
        for i in range(1, g.shape[0]):
            acc = acc + g_ref[i]
        o_ref[...] = acc

    return pl.pallas_call(body, name="sum_devices_%d" % g.shape[1],
                          out_shape=jax.ShapeDtypeStruct(g.shape[1:], F32))(g)


def _sum_windows(g, name):
    n, r, c = g.shape
    tr = 128

    def body(g_ref, o_ref):
        x, y, cc = _place()
        lane0 = (W_IN_SHARD * _slot((x, y, cc))) % DH
        acc = g_ref[0].astype(F32)
        for i in range(1, n):
            acc = acc + g_ref[i].astype(F32)
        o_ref[...] = pltpu.roll(acc, (c - lane0) % c, 1).T

    return pl.pallas_call(
        body, name=name, grid=(r // tr,),
        in_specs=[pl.BlockSpec((n, tr, c), lambda i: (0, i, 0))],
        out_specs=pl.BlockSpec((c, tr), lambda i: (0, i)),
        out_shape=jax.ShapeDtypeStruct((c, r), F32),
        compiler_params=_cp("parallel"),
    )(g)


def _adam_rows(r, c, n):
    budget = 6 * 1024 * 1024
    best = None
    for tr in range(16, r + 1, 16):
        if r % tr == 0 and tr * c * (2 * n + 28) <= budget:
            best = tr
    return best if best is not None else r


def _adamw(g, w, m, v, name):
    n, r, c = g.shape
    tr = _adam_rows(r, c, n)
    bc1 = 1.0 - ADAM_B1 ** ADAM_STEP
    bc2 = 1.0 - ADAM_B2 ** ADAM_STEP

    def body(g_ref, w_ref, m_ref, v_ref, go_ref, d_ref, mo_ref, vo_ref):
        grad = g_ref[0].astype(F32)
        for i in range(1, n):
            grad = grad + g_ref[i].astype(F32)
        go_ref[...] = grad
        m_new = ADAM_B1 * m_ref[...] + (1.0 - ADAM_B1) * grad
        v_new = ADAM_B2 * v_ref[...] + (1.0 - ADAM_B2) * (grad * grad)
        mo_ref[...] = m_new
        vo_ref[...] = v_new
        d_ref[...] = -ADAM_LR * ((m_new / bc1) / (jnp.sqrt(v_new / bc2) + ADAM_EPS) + ADAM_WD * w_ref[...])

    blk = pl.BlockSpec((tr, c), lambda i: (i, 0))
    out = jax.ShapeDtypeStruct((r, c), F32)
    return pl.pallas_call(
        body, name=name, grid=(r // tr,),
        in_specs=[pl.BlockSpec((n, tr, c), lambda i: (0, i, 0)), blk, blk, blk],
        out_specs=[blk] * 4, out_shape=[out] * 4,
        compiler_params=_cp("parallel"),
    )(g, w, m, v)


def kernel(x, c, ctx, c_ctx, w_mod, b_mod, norm_pre1, norm_post1, norm_pre2, norm_post2, w_in, hg_lb, hg_onorm, gla_w_gk, gla_b_gk, gla_onorm, w_br_hg, w_br_gla, w_out, w_ff_gate, w_ff_up, w_ff_down, loss_target, m_c_ctx, m_w_mod, m_b_mod, m_norm_pre1, m_norm_post1, m_norm_pre2, m_norm_post2, m_w_in, m_hg_lb, m_hg_onorm, m_gla_w_gk, m_gla_b_gk, m_gla_onorm, m_w_br_hg, m_w_br_gla, m_w_out, m_w_ff_gate, m_w_ff_up, m_w_ff_down, v_c_ctx, v_w_mod, v_b_mod, v_norm_pre1, v_norm_post1, v_norm_pre2, v_norm_post2, v_w_in, v_hg_lb, v_hg_onorm, v_gla_w_gk, v_gla_b_gk, v_gla_onorm, v_w_br_hg, v_w_br_gla, v_w_out, v_w_ff_gate, v_w_ff_up, v_w_ff_down):
    xi, yi, ci = lax.axis_index("x"), lax.axis_index("y"), lax.axis_index("c")
    me = 4 * xi + 2 * yi + ci
    t = CTX + x.shape[1]

    c_all, lb_g, wgk_g, bgk_g = _all_gather([c, hg_lb, gla_w_gk[0], gla_b_gk[0]], "ag_small")
    tr_ = lambda a: jnp.swapaxes(a[0], 0, 1)
    big = [w_in[0], w_br_hg[0], w_br_gla[0], w_out[0], tr_(w_ff_gate), tr_(w_ff_up), w_ff_down[0]]
    big_bf = [w.astype(BF16) for w in big]
    cols = lambda g: jnp.transpose(g, (1, 0, 2)).reshape(g.shape[1], N_DEV * g.shape[2])

    def get_w_in(after):
        land, = _split_wait(w_in_handle, "ag_w_in_wait", after)
        return _layout_w_in(cols(_forward_to_sibling(land, "ag_w_in_forward")))

    def get_mix(after):
        g_brh, g_brg, g_out = _split_wait(mix_handle, "ag_mix_wait", after)
        return _gate_cols(cols(g_brh)), _gate_cols(cols(g_brg)), _gate_rows(g_out.reshape(D, D))

    def get_ffn(after):
        g_gate, g_up, g_down = _split_wait(ffn_handle, "ag_ffn_wait", after)
        return (g_gate.reshape(D_FF, D), g_up.reshape(D_FF, D)), g_down.reshape(D_FF, D)

    hg_lb_full = jnp.transpose(lb_g, (1, 2, 0, 3)).reshape(2, 2, HW)
    wgk_k = _layout_wgk(jnp.transpose(wgk_g, (1, 2, 0, 3)).reshape(2, 16, HW)).astype(BF16)
    bgk_k = jnp.transpose(bgk_g, (1, 0, 2)).reshape(1, D)
    onw = jnp.concatenate([jnp.tile(hg_onorm, (1, NH // 2)), jnp.tile(gla_onorm, (1, NH // 2))], axis=1)

    n_mod = w_mod.shape[2]
    a9 = jnp.concatenate([c_ctx[None], c_all[:, 0], jnp.zeros((16 - 1 - N_DEV, D), F32)], axis=0)
    b_loc = lax.dynamic_slice(b_mod, (0, me * n_mod), (1, n_mod))
    s_loc = _mod_fwd(a9, w_mod[0], b_loc)
    s_all, = _all_gather([s_loc], "ag_mod")
    mod_all = jnp.transpose(s_all, (1, 0, 2)).reshape(16, N_DEV * n_mod)
    pad8 = lambda m: jnp.concatenate([m.reshape(6, D), jnp.zeros((2, D), F32)], axis=0)
    modc = pad8(mod_all[0])
    modx = pad8(lax.dynamic_slice(mod_all, (1 + me, 0), (1, N_DEV * n_mod))[0])

    gathered = lambda arrs: [(N_DEV,) + a.shape for a in arrs]
    w_in_handle, tok = _split_start(_view_near, gathered(big_bf[:1]), big_bf[:1], "ag_w_in_start", s_all)
    mix_handle, tok = _split_start(_view_whole, gathered(big_bf[1:4]), big_bf[1:4], "ag_mix_start", tok)
    ffn_handle, tok = _split_start(_view_whole, gathered(big_bf[4:]), big_bf[4:], "ag_ffn_start", tok)

    z = _tie(jnp.concatenate([ctx[0], x[0]], axis=0), tok, "tie_z")
    norms = (norm_pre1, norm_post1, norm_pre2, norm_post2)
    shard = lambda d: jnp.transpose(d.reshape(d.shape[0], N_DEV, -1), (1, 0, 2)).astype(BF16)
    rowshard = lambda d: d.reshape(N_DEV, d.shape[0] // N_DEV, d.shape[1]).astype(BF16)
    sent, w_in_grad = [], {}

    def send_w_in(i, x_after):
        half, rows = W_IN_GRAD_CHUNKS[i]
        handle, tok = _split_start(_view_window(rows), [(N_DEV, rows[1] - rows[0], D)], w_in_grad[half],
                                   "grads_w_in%d_start" % i, x_after)
        w_in_grad[half] = handle["srcs"]
        sent.append(("w_in%d" % i, ["w_in#%d" % i], handle))
        return _tie(x_after, tok, "tie_w_in%d" % i)

    def send(names, grads, x_after):
        if names == ("w_in_a",):
            w_in_grad["a"] = list(grads)
            return send_w_in(0, x_after)
        if names == ("w_in_b",):
            w_in_grad["b"] = list(grads)
            return x_after
        arrs, leaves = [], []
        for nm, g in zip(names, grads):
            if nm in ("w_gate_t", "w_up_t"):
                arrs.append(rowshard(g))
                leaves.append({"w_gate_t": "w_ff_gate", "w_up_t": "w_ff_up"}[nm])
            elif nm == "w_down":
                arrs.append(rowshard(g))
                leaves.append("w_ff_down")
            elif nm == "w_out":
                arrs.append(rowshard(g[GOFF:GOFF + D]))
                leaves.append(nm)
            else:
                arrs.append(shard(g[:, GOFF:GOFF + D]))
                leaves.append(nm)
        handle, tok = _split_start(_view_block, [a.shape for a in arrs], arrs, "grads_%s_start" % names[0], x_after)
        sent.append((names[0], leaves, handle))
        return _tie(x_after, tok, "tie_" + names[0])

    r = _local_step(z, loss_target[0], modc, modx, norms, onw, hg_lb_full, wgk_k, bgk_k,
                    get_w_in, get_mix, get_ffn, send)
    grad_x = r["grad_x"][None]

    sm_pre, sm_mid, sm_fin = r["sm_pre"], r["sm_mid"], r["sm_final"]
    dmodc = jnp.stack([sm_pre[0], sm_pre[2], sm_mid[4], sm_mid[0], sm_mid[2], sm_fin[0]]).reshape(-1)
    dmodx = jnp.stack([sm_pre[1], sm_pre[3], sm_mid[5], sm_mid[1], sm_mid[3], sm_fin[1]]).reshape(-1)
    on = r["sm_post"][0].reshape(NH, DH)
    pieces = [dmodc, dmodx, sm_pre[4], sm_mid[7], sm_mid[6], sm_fin[2], on[:NH // 2].sum(0), on[NH // 2:].sum(0),
              r["d_lb"][:2].reshape(-1), _unlayout_wgk(r["d_wgk"]).reshape(-1), r["d_bgk"][0]]
    loss_local = (0.5 / D) * jnp.sum(r["loss_vec"])
    pieces.append(jnp.concatenate([loss_local.reshape(1), jnp.zeros((DH - 1,), F32)]))
    sizes = [p.shape[0] for p in pieces]
    pack = jnp.concatenate(pieces).reshape(-1, DH)
    pack_all, = _all_gather([pack], "ag_small_grads")
    pack_all = send_w_in(1, pack_all)
    tot = _sum_devices(pack_all).reshape(-1)
    offs = [sum(sizes[:i]) for i in range(len(sizes))]
    part = lambda i: tot[offs[i]:offs[i] + sizes[i]]
    dmodc_t, dmodx_t = part(0), part(1)
    g_b_mod = (dmodc_t + dmodx_t)[None]
    g_norms = [part(i)[None] for i in (2, 3, 4, 5)]
    g_hg_on, g_gla_on = part(6)[None], part(7)[None]
    lb0 = lax.dynamic_slice(part(8).reshape(2, HW), (0, me * (HW // N_DEV)), (2, HW // N_DEV))
    g_hg_lb = jnp.stack([lb0, -lb0])
    g_wgk = lax.dynamic_slice(part(9).reshape(2, 16, HW), (0, 0, me * (HW // N_DEV)), (2, 16, HW // N_DEV))[None]
    g_bgk = lax.dynamic_slice(part(10).reshape(2, HW), (0, me * (HW // N_DEV)), (2, HW // N_DEV))[None]
    loss = part(11)[0]

    dmx_all = pack_all.reshape(N_DEV, -1)[:, sizes[0]:sizes[0] + sizes[1]]
    d9 = jnp.concatenate([lax.dynamic_slice(dmodc_t[None], (0, me * n_mod), (1, n_mod)),
                          lax.dynamic_slice(dmx_all, (0, me * n_mod), (N_DEV, n_mod)),
                          jnp.zeros((16 - 1 - N_DEV, n_mod), F32)], axis=0)
    g_w_mod, dcc_part = _mod_bwd(a9, d9, w_mod[0])
    dcc_all, = _all_gather([dcc_part], "ag_c_ctx")
    dcc_all = send_w_in(2, dcc_all)
    g_c_ctx = _sum_devices(dcc_all)[0]

    recv = {}
    for first, leaves, handle in sent:
        if not first.startswith("w_in"):
            recv.update(zip(leaves, _split_wait(handle, "grads_%s_wait" % first, g_c_ctx)))
    moms = [(m_w_in, v_w_in), (m_w_br_hg, v_w_br_hg), (m_w_br_gla, v_w_br_gla), (m_w_out, v_w_out),
            (m_w_ff_gate, v_w_ff_gate), (m_w_ff_up, v_w_ff_up), (m_w_ff_down, v_w_ff_down)]
    names = ["w_in", "w_br_hg", "w_br_gla", "w_out", "w_ff_gate", "w_ff_up", "w_ff_down"]
    res = {}

    def update(nm, w, m, v):
        if nm in ("w_ff_gate", "w_ff_up"):
            outs = _adamw(recv[nm], w, tr_(m), tr_(v), "adamw_" + nm)
            res[nm] = [jnp.swapaxes(o, 0, 1)[None] for o in outs]
        else:
            res[nm] = [o[None] for o in _adamw(recv[nm], w, m[0], v[0], "adamw_" + nm)]

    for nm, w, (m, v) in list(zip(names, big, moms))[1:]:
        update(nm, w, m, v)
    res["w_mod"] = [o[None] for o in _adamw(g_w_mod[None], w_mod[0], m_w_mod[0], v_w_mod[0], "adamw_w_mod")]

    small = [("c_ctx", c_ctx, m_c_ctx, v_c_ctx, g_c_ctx), ("b_mod", b_mod, m_b_mod, v_b_mod, g_b_mod),
             ("norm_pre1", norm_pre1, m_norm_pre1, v_norm_pre1, g_norms[0]),
             ("norm_post1", norm_post1, m_norm_post1, v_norm_post1, g_norms[1]),
             ("norm_pre2", norm_pre2, m_norm_pre2, v_norm_pre2, g_norms[2]),
             ("norm_post2", norm_post2, m_norm_post2, v_norm_post2, g_norms[3]),
             ("hg_lb", hg_lb, m_hg_lb, v_hg_lb, g_hg_lb), ("hg_onorm", hg_onorm, m_hg_onorm, v_hg_onorm, g_hg_on),
             ("gla_w_gk", gla_w_gk, m_gla_w_gk, v_gla_w_gk, g_wgk), ("gla_b_gk", gla_b_gk, m_gla_b_gk, v_gla_b_gk, g_bgk),
             ("gla_onorm", gla_onorm, m_gla_onorm, v_gla_onorm, g_gla_on)]
    flat = lambda k: jnp.concatenate([s[k].reshape(-1) for s in small]).reshape(-1, DH)
    outs = _adamw(flat(4)[None], flat(1), flat(2), flat(3), "adamw_small")
    off = 0
    for nm, w, _, _, _ in small:
        res[nm] = [o.reshape(-1)[off:off + w.size].reshape(w.shape) for o in outs]
        off += w.size

    done = [res[nm][0] for nm in names[1:]] + [res["w_mod"][0], outs[0]]
    sums = []
    for i, (first, leaves, handle) in enumerate(s for s in sent if s[0].startswith("w_in")):
        half = W_IN_GRAD_CHUNKS[i][0]
        land, = _split_wait(handle, "grads_%s_wait" % first, done, srcs=w_in_grad[half])
        w_in_grad[half] = handle["srcs"]
        sums.append(_sum_windows(land, "sum_windows%d" % i))
    g_t = jnp.concatenate(sums, axis=1)[:W_IN_SHARD]
    lin = lambda a: a.reshape(W_IN_SHARD * D // DH, DH)
    major = lambda a: lin(jnp.transpose(a, (2, 0, 1)))
    outs = _adamw(lin(g_t)[None], major(w_in), major(m_w_in), major(v_w_in), "adamw_w_in")
    res["w_in"] = [jnp.transpose(o.reshape(W_IN_SHARD, 1, D), (1, 2, 0)) for o in outs]

    order = ["c_ctx", "w_mod", "b_mod", "norm_pre1", "norm_post1", "norm_pre2", "norm_post2", "w_in", "hg_lb",
             "hg_onorm", "gla_w_gk", "gla_b_gk", "gla_onorm", "w_br_hg", "w_br_gla", "w_out", "w_ff_gate", "w_ff_up",
             "w_ff_down"]
    return (loss, grad_x, *[res[n][k] for k in range(4) for n in order])
```

```python
import functools

import jax
import jax.numpy as jnp
from jax import lax
from jax.experimental import pallas as pl
from jax.experimental.pallas import tpu as pltpu

F32 = jnp.float32
BF16 = jnp.bfloat16
HI = lax.Precision.HIGHEST

N_DEV = 8
D = 1024
CTX = 256
HW = 512
DH = 128
NH = 8
D_FF = 2816
EPS = 1e-6
GLA_NORM = 16.0
CHUNK = 64
TR = 256
NCT = CTX // TR
W_IN_COLS = 7168
MAIN0 = 0
LR0 = 4608
GW = 1152
GOFF = 32
GATE_HG0 = LR0
GATE_GLA0 = LR0 + D
LEVELS = (32, 16, 8)
EXP_CLAMP = 80.0
VMEM_LIMIT = 48 * 1024 * 1024

ADAM_LR, ADAM_B1, ADAM_B2, ADAM_EPS, ADAM_WD, ADAM_STEP = 0.001, 0.9, 0.999, 1e-08, 0.01, 10


def _cp(*sem):
    return pltpu.CompilerParams(dimension_semantics=sem, vmem_limit_bytes=VMEM_LIMIT)


def _sig(x):
    return jax.nn.sigmoid(x)


def _silu(x):
    return x * _sig(x)


def _dsilu(x):
    s = _sig(x)
    return s * (1.0 + x * (1.0 - s))


def _rstd(x):
    return lax.rsqrt(jnp.mean(x * x, axis=-1, keepdims=True) + EPS)


def _rms_bwd(a, y, r):
    return r * (a - y * (r * r) * jnp.mean(a * y, axis=-1, keepdims=True))


def _colsum(x):
    return jnp.sum(x, axis=0, keepdims=True)


def _dot(a, b, dims, precision=None):
    return lax.dot_general(a, b, (dims, ((), ())), preferred_element_type=F32, precision=precision)


NN = ((1,), (0,))
NT = ((1,), (1,))
TN = ((0,), (0,))

SCAN_HEADS_FWD = 4
SCAN_HEADS_BWD = 4


def _split_dot(m, x):
    mb = m.astype(BF16)
    x1 = x.astype(BF16)
    r1 = x - x1.astype(F32)
    x2 = r1.astype(BF16)
    x3 = (r1 - x2.astype(F32)).astype(BF16)
    return _dot(mb, x1, NN) + _dot(mb, x2, NN) + _dot(mb, x3, NN)


def _matmul(a, b, dims, out_dtype, name, tm, tn, tk, a_off=0, m_out=None):
    pair = isinstance(b, (tuple, list))
    bs = list(b) if pair else [b]
    b1 = bs[0]
    rows = b1.shape[0] * len(bs)
    half = None
    if dims == NN:
        m, k, n = a.shape[0], rows, b1.shape[1]
        a_spec = pl.BlockSpec((tm, tk), lambda i, j, kk: (i, kk + a_off))
        half = b1.shape[0] // tk
        b_maps = [lambda i, j, kk: (kk, j)] if not pair else [
            lambda i, j, kk: (jnp.minimum(kk, half - 1), j), lambda i, j, kk: (jnp.maximum(kk - half, 0), j)]
        b_specs = [pl.BlockSpec((tk, tn), f) for f in b_maps]
        axis = 2
    elif dims == NT:
        m, k, n = a.shape[0], b1.shape[1], rows
        a_spec = pl.BlockSpec((tm, tk), lambda i, j, kk: (i, kk + a_off))
        half = b1.shape[0] // tn
        b_maps = [lambda i, j, kk: (j, kk)] if not pair else [
            lambda i, j, kk: (jnp.minimum(j, half - 1), kk), lambda i, j, kk: (jnp.maximum(j - half, 0), kk)]
        b_specs = [pl.BlockSpec((tn, tk), f) for f in b_maps]
        axis = 1
    else:
        assert not pair
        m, k = (a.shape[1] if m_out is None else m_out), a.shape[0]
        n = b1.shape[1]
        a_spec = pl.BlockSpec((tk, tm), lambda i, j, kk: (kk, i + a_off))
        b_specs = [pl.BlockSpec((tk, tn), lambda i, j, kk: (kk, j))]
    assert m % tm == 0 and n % tn == 0 and k % tk == 0, (name, m, n, k, tm, tn, tk)
    nk = k // tk
    nb = len(bs)

    def body(a_ref, *refs):
        o_ref = refs[nb]
        if pair:
            bv = jnp.where(pl.program_id(axis) < half, refs[0][...], refs[1][...])
        else:
            bv = refs[0][...]
        part = _dot(a_ref[...], bv, dims)
        if nk == 1:
            o_ref[...] = part.astype(o_ref.dtype)
            return
        acc_ref = refs[nb + 1]
        kk = pl.program_id(2)

        @pl.when(kk == 0)
        def _():
            acc_ref[...] = part

        @pl.when(kk > 0)
        def _():
            acc_ref[...] += part

        @pl.when(kk == nk - 1)
        def _():
            o_ref[...] = acc_ref[...].astype(o_ref.dtype)

    return pl.pallas_call(
        body,
        name=name,
        grid=(m // tm, n // tn, nk),
        in_specs=[a_spec] + b_specs,
        out_specs=pl.BlockSpec((tm, tn), lambda i, j, kk: (i, j)),
        out_shape=jax.ShapeDtypeStruct((m, n), out_dtype),
        scratch_shapes=[] if nk == 1 else [pltpu.VMEM((tm, tn), F32)],
        compiler_params=_cp("parallel", "parallel", "arbitrary"),
    )(a, *bs)


def _row(c):
    return pl.BlockSpec((TR, c), lambda i: (i, 0))


def _rowcol(width, cb):
    return pl.BlockSpec((TR, width), lambda i: (i, cb))


def _full(shape):
    return pl.BlockSpec(shape, lambda i: (0,) * len(shape))


def _mod_row(mc_ref, mx_ref, k, is_ctx):
    return jnp.where(is_ctx, mc_ref[k:k + 1, :], mx_ref[k:k + 1, :])


def _acc_row(ref, k, val):
    ref[k:k + 1, :] += val


def _acc_mod(ref, k, is_ctx, val):
    zero = jnp.zeros_like(val)
    ref[k:k + 1, :] += jnp.where(is_ctx, val, zero)
    ref[k + 1:k + 2, :] += jnp.where(is_ctx, zero, val)


def _prenorm(z, nw, modc, modx, i_shift, i_scale, name):
    t = z.shape[0]

    def body(z_ref, nw_ref, mc_ref, mx_ref, h_ref):
        is_ctx = pl.program_id(0) < NCT
        x = z_ref[...]
        n = x * _rstd(x) * nw_ref[...]
        h = n * (1.0 + _mod_row(mc_ref, mx_ref, i_scale, is_ctx)) + _mod_row(mc_ref, mx_ref, i_shift, is_ctx)
        h_ref[...] = h.astype(BF16)

    return pl.pallas_call(
        body, name=name, grid=(t // TR,),
        in_specs=[_row(D), _full((1, D)), _full((8, D)), _full((8, D))],
        out_specs=_row(D),
        out_shape=jax.ShapeDtypeStruct((t, D), BF16),
        compiler_params=_cp("parallel"),
    )(z, nw, modc, modx)


def _hg_lb(lb_ref, d):
    a0 = lb_ref[0, d:d + 1, :]
    a1 = lb_ref[1, d:d + 1, :]
    mx = jnp.maximum(a0, a1)
    e0 = jnp.exp(a0 - mx)
    e1 = jnp.exp(a1 - mx)
    return e0 / (e0 + e1)


def _log_sigmoid(x):
    return jnp.minimum(x, 0.0) - jnp.log(1.0 + jnp.exp(-jnp.abs(x)))


def _gates_fwd(p, hg_lb, wgk, bgk):
    t = p.shape[0]
    seg = lambda j: _rowcol(HW, MAIN0 // HW + j)

    def body(hq_ref, hi_ref, hf_ref, hb_ref, gq_ref, gk_ref, gv_ref, lr_ref, lb_ref, wgk_ref, bgk_ref,
             q_ref, v_ref, kf_ref, kb_ref, gf_ref, gb_ref):
        q_ref[:, :HW] = _silu(hq_ref[...].astype(F32)).astype(BF16)
        q_ref[:, HW:] = (gq_ref[...].astype(F32) * (DH ** -0.5)).astype(BF16)
        v_ref[:, :HW] = hi_ref[...]
        v_ref[:, HW:] = gv_ref[...]
        xg = _dot(lr_ref[...].astype(BF16), wgk_ref[...], NN) + bgk_ref[...]
        for d, (raw_ref, k_ref, g_ref) in enumerate(((hf_ref, kf_ref, gf_ref), (hb_ref, kb_ref, gb_ref))):
            lbd = _hg_lb(lb_ref, d)
            f = lbd + (1.0 - lbd) * _sig(raw_ref[...].astype(F32))
            k_ref[:, :HW] = (1.0 - f).astype(BF16)
            k_ref[:, HW:] = gk_ref[...]
            g_ref[:, :HW] = jnp.log(f)
            g_ref[:, HW:] = _log_sigmoid(xg[:, d * HW:(d + 1) * HW]) * (1.0 / GLA_NORM)

    out = jax.ShapeDtypeStruct((t, D), F32)
    outb = jax.ShapeDtypeStruct((t, D), BF16)
    return pl.pallas_call(
        body, name="gates_fwd", grid=(t // TR,),
        in_specs=[seg(0), seg(1), seg(2), seg(3), seg(5), seg(6), seg(7), _rowcol(DH, LR0 // DH),
                  _full((2, 2, HW)), _full((DH, D)), _full((1, D))],
        out_specs=[_row(D)] * 6,
        out_shape=[outb] * 4 + [out] * 2,
        compiler_params=_cp("parallel"),
    )(p, p, p, p, p, p, p, p, hg_lb, wgk, bgk)


def _post_fwd(o_fw, o_bw, p, onw):
    t = o_fw.shape[0]

    def body(of_ref, ob_ref, g1_ref, g2_ref, w_ref, y_ref):
        for h in range(NH):
            sl = slice(h * DH, (h + 1) * DH)
            o = of_ref[:, sl] + ob_ref[:, sl]
            g_ref = g1_ref if h < NH // 2 else g2_ref
            gs = slice((h % (NH // 2)) * DH, (h % (NH // 2) + 1) * DH)
            n = o * _rstd(o) * w_ref[:, sl]
            y_ref[:, sl] = (n * _silu(g_ref[:, gs].astype(F32))).astype(BF16)

    return pl.pallas_call(
        body, name="post_fwd", grid=(t // TR,),
        in_specs=[_row(D), _row(D), _rowcol(HW, MAIN0 // HW + 4), _rowcol(HW, MAIN0 // HW + 8), _full((1, D))],
        out_specs=_row(D),
        out_shape=jax.ShapeDtypeStruct((t, D), BF16),
        compiler_params=_cp("parallel"),
    )(o_fw, o_bw, p, p, onw)


def _gate_window_specs(col0):
    return [_rowcol(HW, col0 // HW), _rowcol(HW, col0 // HW + 1), _rowcol(DH, (col0 + 2 * HW) // DH)]


def _gate_window(refs):
    return jnp.concatenate([r[...].astype(F32) for r in refs], axis=1)


def _merge_fwd(p, u1, u2):
    t = p.shape[0]

    def body(a0, a1, a2, b0, b1, b2, u1_ref, u2_ref, m_ref):
        f = lambda r: r[...].astype(F32)
        m_ref[...] = (_sig(_gate_window((a0, a1, a2))) * f(u1_ref)
                      + _sig(_gate_window((b0, b1, b2))) * f(u2_ref)).astype(BF16)

    return pl.pallas_call(
        body, name="merge_fwd", grid=(t // TR,),
        in_specs=_gate_window_specs(GATE_HG0) + _gate_window_specs(GATE_GLA0) + [_row(GW), _row(GW)],
        out_specs=_row(GW),
        out_shape=jax.ShapeDtypeStruct((t, GW), BF16),
        compiler_params=_cp("parallel"),
    )(p, p, p, p, p, p, u1, u2)


def _mid_fwd(z, y1, nw_post, nw_pre, modc, modx):
    t = z.shape[0]

    def body(z_ref, y_ref, wpo_ref, wpr_ref, mc_ref, mx_ref, z1_ref, h_ref):
        is_ctx = pl.program_id(0) < NCT
        y = y_ref[...].astype(F32)
        z1 = z_ref[...] + _mod_row(mc_ref, mx_ref, 2, is_ctx) * (y * _rstd(y) * wpo_ref[...])
        z1_ref[...] = z1
        n = z1 * _rstd(z1) * wpr_ref[...]
        h = n * (1.0 + _mod_row(mc_ref, mx_ref, 4, is_ctx)) + _mod_row(mc_ref, mx_ref, 3, is_ctx)
        h_ref[...] = h.astype(BF16)

    return pl.pallas_call(
        body, name="mid_fwd", grid=(t // TR,),
        in_specs=[_row(D), _row(D), _full((1, D)), _full((1, D)), _full((8, D)), _full((8, D))],
        out_specs=[_row(D), _row(D)],
        out_shape=[jax.ShapeDtypeStruct((t, D), F32), jax.ShapeDtypeStruct((t, D), BF16)],
        compiler_params=_cp("parallel"),
    )(z, y1, nw_post, nw_pre, modc, modx)


def _swiglu_fwd(uv):
    t = uv.shape[0]

    def body(u_ref, v_ref, a_ref):
        a_ref[...] = (_silu(u_ref[...].astype(F32)) * v_ref[...].astype(F32)).astype(BF16)

    return pl.pallas_call(
        body, name="swiglu_fwd", grid=(t // TR,),
        in_specs=[_rowcol(D_FF, 0), _rowcol(D_FF, 1)],
        out_specs=_row(D_FF),
        out_shape=jax.ShapeDtypeStruct((t, D_FF), BF16),
        compiler_params=_cp("parallel"),
    )(uv, uv)


def _swiglu_bwd(uv, da):
    t = uv.shape[0]

    def body(u_ref, v_ref, da_ref, d_ref):
        u = u_ref[...].astype(F32)
        d = da_ref[...].astype(F32)
        d_ref[:, :D_FF] = (d * v_ref[...].astype(F32) * _dsilu(u)).astype(BF16)
        d_ref[:, D_FF:] = (d * _silu(u)).astype(BF16)

    return pl.pallas_call(
        body, name="swiglu_bwd", grid=(t // TR,),
        in_specs=[_rowcol(D_FF, 0), _rowcol(D_FF, 1), _row(D_FF)],
        out_specs=_row(2 * D_FF),
        out_shape=jax.ShapeDtypeStruct((t, 2 * D_FF), BF16),
        compiler_params=_cp("parallel"),
    )(uv, uv, da)


def _final(z1, y2, target, nw, modc, modx):
    t = z1.shape[0]

    def body(z1_ref, y_ref, tg_ref, w_ref, mc_ref, mx_ref, dz_ref, dy_ref, loss_ref, sm_ref):
        i = pl.program_id(0)
        is_ctx = i < NCT

        @pl.when(i == 0)
        def _():
            loss_ref[...] = jnp.zeros_like(loss_ref)
            sm_ref[...] = jnp.zeros_like(sm_ref)

        g = _mod_row(mc_ref, mx_ref, 5, is_ctx)
        y = y_ref[...].astype(F32)
        r = _rstd(y)
        w = w_ref[...]
        yr = y * r
        n = yr * w
        e = z1_ref[...] + g * n - tg_ref[...]
        lat = jnp.where(is_ctx, 0.0, 1.0)
        loss_ref[...] += lat * _colsum(e * e)
        dz = e * (lat / D)
        dz_ref[...] = dz
        _acc_mod(sm_ref, 0, is_ctx, _colsum(dz * n))
        dn = dz * g
        _acc_row(sm_ref, 2, _colsum(dn * yr))
        dy_ref[...] = _rms_bwd(dn * w, y, r).astype(BF16)

    return pl.pallas_call(
        body, name="final", grid=(t // TR,),
        in_specs=[_row(D), _row(D), pl.BlockSpec((TR, D), lambda i: (jnp.maximum(i - NCT, 0), 0)),
                  _full((1, D)), _full((8, D)), _full((8, D))],
        out_specs=[_row(D), _row(D), _full((1, D)), _full((8, D))],
        out_shape=[jax.ShapeDtypeStruct((t, D), F32), jax.ShapeDtypeStruct((t, D), BF16),
                   jax.ShapeDtypeStruct((1, D), F32), jax.ShapeDtypeStruct((8, D), F32)],
        compiler_params=_cp("arbitrary"),
    )(z1, y2, target, nw, modc, modx)


def _mid_bwd(dh2, dz, z, z1, y1, nw_post, nw_pre, modc, modx):
    t = z.shape[0]

    def body(dh_ref, dz_ref, z_ref, z1_ref, y_ref, wpo_ref, wpr_ref, mc_ref, mx_ref, dzo_ref, dy_ref, sm_ref):
        i = pl.program_id(0)
        is_ctx = i < NCT

        @pl.when(i == 0)
        def _():
            sm_ref[...] = jnp.zeros_like(sm_ref)

        dh = dh_ref[...].astype(F32)
        z1 = z1_ref[...]
        r = _rstd(z1)
        zr = z1 * r
        wpr = wpr_ref[...]
        n = zr * wpr
        _acc_mod(sm_ref, 0, is_ctx, _colsum(dh))
        _acc_mod(sm_ref, 2, is_ctx, _colsum(dh * n))
        dn = dh * (1.0 + _mod_row(mc_ref, mx_ref, 4, is_ctx))
        _acc_row(sm_ref, 6, _colsum(dn * zr))
        dz1 = dz_ref[...] + _rms_bwd(dn * wpr, z1, r)
        dzo_ref[...] = dz1
        y = y_ref[...].astype(F32)
        r1 = _rstd(y)
        yr = y * r1
        wpo = wpo_ref[...]
        g = _mod_row(mc_ref, mx_ref, 2, is_ctx)
        _acc_mod(sm_ref, 4, is_ctx, _colsum(dz1 * (yr * wpo)))
        dn1 = dz1 * g
        _acc_row(sm_ref, 7, _colsum(dn1 * yr))
        dy_ref[...] = _rms_bwd(dn1 * wpo, y, r1).astype(BF16)

    return pl.pallas_call(
        body, name="mid_bwd", grid=(t // TR,),
        in_specs=[_row(D)] * 5 + [_full((1, D)), _full((1, D)), _full((8, D)), _full((8, D))],
        out_specs=[_row(D), _row(D), _full((8, D))],
        out_shape=[jax.ShapeDtypeStruct((t, D), F32), jax.ShapeDtypeStruct((t, D), BF16),
                   jax.ShapeDtypeStruct((8, D), F32)],
        compiler_params=_cp("arbitrary"),
    )(dh2, dz, z, z1, y1, nw_post, nw_pre, modc, modx)


def _pre_bwd(dh1, dz, z, nw, modc, modx):
    t = z.shape[0]

    def body(dh_ref, dz_ref, z_ref, w_ref, mc_ref, mx_ref, dzo_ref, sm_ref):
        i = pl.program_id(0)
        is_ctx = i < NCT

        @pl.when(i == 0)
        def _():
            sm_ref[...] = jnp.zeros_like(sm_ref)

        dh = dh_ref[...].astype(F32)
        x = z_ref[...]
        r = _rstd(x)
        xr = x * r
        w = w_ref[...]
        _acc_mod(sm_ref, 0, is_ctx, _colsum(dh))
        _acc_mod(sm_ref, 2, is_ctx, _colsum(dh * (xr * w)))
        dn = dh * (1.0 + _mod_row(mc_ref, mx_ref, 1, is_ctx))
        _acc_row(sm_ref, 4, _colsum(dn * xr))
        dzo_ref[...] = dz_ref[...] + _rms_bwd(dn * w, x, r)

    return pl.pallas_call(
        body, name="pre_bwd", grid=(t // TR,),
        in_specs=[_row(D)] * 3 + [_full((1, D)), _full((8, D)), _full((8, D))],
        out_specs=[pl.BlockSpec((TR, D), lambda i: (jnp.maximum(i - NCT, 0), 0)), _full((8, D))],
        out_shape=[jax.ShapeDtypeStruct((t - CTX, D), F32), jax.ShapeDtypeStruct((8, D), F32)],
        compiler_params=_cp("arbitrary"),
    )(dh1, dz, z, nw, modc, modx)


def _merge_bwd(dm, p, u1, u2):
    t = dm.shape[0]

    def body(dm_ref, a0, a1, a2, b0, b1, b2, u1_ref, u2_ref, du1_ref, du2_ref, dg_ref):
        dm_ = dm_ref[...].astype(F32)
        s1 = _sig(_gate_window((a0, a1, a2)))
        s2 = _sig(_gate_window((b0, b1, b2)))
        du1_ref[...] = (dm_ * s1).astype(BF16)
        du2_ref[...] = (dm_ * s2).astype(BF16)
        dg_ref[:, :GW] = (dm_ * u1_ref[...].astype(F32) * s1 * (1.0 - s1)).astype(BF16)
        dg_ref[:, GW:] = (dm_ * u2_ref[...].astype(F32) * s2 * (1.0 - s2)).astype(BF16)

    return pl.pallas_call(
        body, name="merge_bwd", grid=(t // TR,),
        in_specs=[_row(GW)] + _gate_window_specs(GATE_HG0) + _gate_window_specs(GATE_GLA0) + [_row(GW), _row(GW)],
        out_specs=[_row(GW), _row(GW), _row(2 * GW)],
        out_shape=[jax.ShapeDtypeStruct((t, GW), BF16), jax.ShapeDtypeStruct((t, GW), BF16),
                   jax.ShapeDtypeStruct((t, 2 * GW), BF16)],
        compiler_params=_cp("parallel"),
    )(dm, p, p, p, p, p, p, u1, u2)


def _post_bwd(dy_hg, dy_gla, o_fw, o_bw, p, onw):
    t = o_fw.shape[0]

    def body(d1_ref, d2_ref, of_ref, ob_ref, g1_ref, g2_ref, w_ref, do_ref, dg_ref, sm_ref):
        @pl.when(pl.program_id(0) == 0)
        def _():
            sm_ref[...] = jnp.zeros_like(sm_ref)

        for h in range(NH):
            sl = slice(h * DH, (h + 1) * DH)
            gs = slice((h % (NH // 2)) * DH, (h % (NH // 2) + 1) * DH)
            g_ref, d_ref = (g1_ref, d1_ref) if h < NH // 2 else (g2_ref, d2_ref)
            o = of_ref[:, sl] + ob_ref[:, sl]
            r = _rstd(o)
            orr = o * r
            w = w_ref[:, sl]
            gt = g_ref[:, gs].astype(F32)
            dy = d_ref[:, gs].astype(F32)
            dg_ref[:, sl] = (dy * (orr * w) * _dsilu(gt)).astype(BF16)
            dn = dy * _silu(gt)
            sm_ref[0:1, sl] += _colsum(dn * orr)
            do_ref[:, sl] = _rms_bwd(dn * w, o, r)

    return pl.pallas_call(
        body, name="post_bwd", grid=(t // TR,),
        in_specs=[_row(HW), _row(HW), _row(D), _row(D), _rowcol(HW, MAIN0 // HW + 4), _rowcol(HW, MAIN0 // HW + 8),
                  _full((1, D))],
        out_specs=[_row(D), _row(D), _full((8, D))],
        out_shape=[jax.ShapeDtypeStruct((t, D), F32), jax.ShapeDtypeStruct((t, D), BF16),
                   jax.ShapeDtypeStruct((8, D), F32)],
        compiler_params=_cp("arbitrary"),
    )(dy_hg, dy_gla, o_fw, o_bw, p, p, onw)


def _gates_bwd(p, hg_lb, wgk, bgk, dgm, dgo, dq_f, dq_b, dv_f, dv_b, dk_f, dk_b, dg_f, dg_b):
    t = p.shape[0]
    seg = lambda j: _rowcol(HW, MAIN0 // HW + j)

    def body(hq_ref, hf_ref, hb_ref, lr_ref, lb_ref, wgk_ref, bgk_ref, dgm_ref, dgo_ref,
             dqf_ref, dqb_ref, dvf_ref, dvb_ref, dkf_ref, dkb_ref, dgf_ref, dgb_ref,
             dp_ref, dlb_ref, dw_ref, db_ref):
        @pl.when(pl.program_id(0) == 0)
        def _():
            dlb_ref[...] = jnp.zeros_like(dlb_ref)
            dw_ref[...] = jnp.zeros_like(dw_ref)
            db_ref[...] = jnp.zeros_like(db_ref)

        c0 = MAIN0

        def put(j, val):
            dp_ref[:, c0 + j * HW:c0 + (j + 1) * HW] = val.astype(BF16)

        dq = dqf_ref[...].astype(F32) + dqb_ref[...].astype(F32)
        dv = dvf_ref[...].astype(F32) + dvb_ref[...].astype(F32)
        put(0, dq[:, :HW] * _dsilu(hq_ref[...].astype(F32)))
        put(1, dv[:, :HW])
        put(5, dq[:, HW:] * (DH ** -0.5))
        put(7, dv[:, HW:])
        put(6, dkf_ref[:, HW:].astype(F32) + dkb_ref[:, HW:].astype(F32))
        dp_ref[:, c0 + 4 * HW:c0 + 5 * HW] = dgo_ref[:, :HW]
        dp_ref[:, c0 + 8 * HW:c0 + 9 * HW] = dgo_ref[:, HW:]
        lr = lr_ref[...].astype(BF16)
        xg = _dot(lr, wgk_ref[...], NN) + bgk_ref[...]
        dxg = []
        for d, (raw_ref, dk_ref, dg_ref) in enumerate(((hf_ref, dkf_ref, dgf_ref), (hb_ref, dkb_ref, dgb_ref))):
            lbd = _hg_lb(lb_ref, d)
            s = _sig(raw_ref[...].astype(F32))
            f = lbd + (1.0 - lbd) * s
            df = dg_ref[:, :HW] / f - dk_ref[:, :HW].astype(F32)
            put(2 + d, df * (1.0 - lbd) * s * (1.0 - s))
            dlb_ref[d:d + 1, :] += _colsum(df * (1.0 - s)) * (lbd * (1.0 - lbd))
            dxg.append(dg_ref[:, HW:] * (1.0 / GLA_NORM) * _sig(-xg[:, d * HW:(d + 1) * HW]))
        dxg = jnp.concatenate(dxg, axis=1)
        db_ref[0:1, :] += _colsum(dxg)
        dxg_b = dxg.astype(BF16)
        dw_ref[...] += _dot(lr, dxg_b, TN)
        dlr = _dot(dxg_b, wgk_ref[...], NT)
        dp_ref[:, LR0:LR0 + DH] = (dlr + dgm_ref[:, :DH].astype(F32)).astype(BF16)
        dp_ref[:, LR0 + DH:GATE_GLA0] = dgm_ref[:, DH:D]
        dp_ref[:, GATE_GLA0:GATE_GLA0 + DH] = dgm_ref[:, D:GW] + dgm_ref[:, GW:GW + DH]
        dp_ref[:, GATE_GLA0 + DH:GATE_GLA0 + GW] = dgm_ref[:, GW + DH:]
        dp_ref[:, GATE_GLA0 + GW:] = jnp.zeros((TR, W_IN_COLS - GATE_GLA0 - GW), BF16)

    return pl.pallas_call(
        body, name="gates_bwd", grid=(t // TR,),
        in_specs=[seg(0), seg(2), seg(3), _rowcol(DH, LR0 // DH), _full((2, 2, HW)), _full((DH, D)), _full((1, D)),
                  _row(2 * GW), _row(D)] + [_row(D)] * 8,
        out_specs=[_row(W_IN_COLS), _full((8, HW)), _full((DH, D)), _full((8, D))],
        out_shape=[jax.ShapeDtypeStruct((t, W_IN_COLS), BF16), jax.ShapeDtypeStruct((8, HW), F32),
                   jax.ShapeDtypeStruct((DH, D), F32), jax.ShapeDtypeStruct((8, D), F32)],
        compiler_params=_cp("arbitrary"),
    )(p, p, p, p, hg_lb, wgk, bgk, dgm, dgo, dq_f, dq_b, dv_f, dv_b, dk_f, dk_b, dg_f, dg_b)


def _scan_consts(rev):
    r = lax.broadcasted_iota(jnp.int32, (CHUNK, CHUNK), 0)
    u = lax.broadcasted_iota(jnp.int32, (CHUNK, CHUNK), 1)
    rp = lax.broadcasted_iota(jnp.int32, (CHUNK, 1), 0)
    if rev:
        r, u, rp = CHUNK - 1 - r, CHUNK - 1 - u, CHUNK - 1 - rp
    tri = jnp.where(u <= r, 1.0, 0.0).astype(F32)
    tri_t = jnp.where(r <= u, 1.0, 0.0).astype(F32)
    lv = []
    for b in LEVELS:
        sh = b.bit_length() - 1
        pair = ((r >> sh) == (u >> sh) + 1) & (((u >> sh) & 1) == 0)
        pair_t = ((u >> sh) == (r >> sh) + 1) & (((r >> sh) & 1) == 0)
        tside = ((rp >> sh) & 1) == 1
        lv.append((pair, pair_t, tside))
    bd = LEVELS[-1].bit_length() - 1
    diag = ((r >> bd) == (u >> bd)) & (u <= r)
    diag_t = ((r >> bd) == (u >> bd)) & (r <= u)
    return tri, tri_t, lv, diag, diag_t


def _row_of(pos, rev):
    return CHUNK - 1 - pos if rev else pos


def _chunk_terms(cum, b_scr, consts, rev):
    _, _, lv, _, _ = consts
    terms = []
    for b, (_, _, tside) in zip(LEVELS, lv):
        pieces = []
        for j in range(CHUNK // (2 * b)):
            row = _row_of(2 * b * j + b - 1, rev)
            pieces.append(jnp.broadcast_to(b_scr[row:row + 1, :], (2 * b, DH)))
        if rev:
            pieces = pieces[::-1]
        bnd = pieces[0] if len(pieces) == 1 else jnp.concatenate(pieces, axis=0)
        w = jnp.exp(jnp.minimum(jnp.where(tside, cum - bnd, bnd - cum), 0.0))
        wq = jnp.where(tside, w, 0.0)
        wk = jnp.where(tside, 0.0, w)
        terms.append((wq, wk))
    b = LEVELS[-1]
    pieces = []
    for j in range(CHUNK // b):
        if j == 0:
            pieces.append(jnp.zeros((b, DH), F32))
        else:
            row = _row_of(b * j - 1, rev)
            pieces.append(jnp.broadcast_to(b_scr[row:row + 1, :], (b, DH)))
    if rev:
        pieces = pieces[::-1]
    start = jnp.concatenate(pieces, axis=0)
    wq = jnp.exp(jnp.minimum(cum - start, 0.0))
    wk = jnp.exp(jnp.minimum(start - cum, EXP_CLAMP))
    terms.append((wq, wk))
    return terms


def _run_staged(units):
    live = list(units)
    while live:
        nxt = []
        for u in live:
            try:
                next(u)
                nxt.append(u)
            except StopIteration:
                pass
        live = nxt


SCAN_TB = 256
SCAN_CB = SCAN_TB // CHUNK


def _block_order(i, ntb, rev):
    nctx = CTX // SCAN_TB
    if not rev:
        return i
    return jnp.where(i < nctx, nctx - 1 - i, ntb - 1 - (i - nctx))


def _chunk_in_block(j, rev):
    return SCAN_CB - 1 - j if rev else j


def _scan_fwd(q, k, v, g, rev):
    t = q.shape[0]
    nc = t // CHUNK
    hpb = SCAN_HEADS_FWD

    def body(q_ref, k_ref, v_ref, g_ref, o_ref, st_ref, s_scr, b_scr):
        consts = _scan_consts(rev)
        _, _, lv, diag, _ = consts
        masks = [pair for pair, _, _ in lv] + [diag]

        @pl.when(pl.program_id(1) == 0)
        def _():
            s_scr[...] = jnp.zeros_like(s_scr)

        tri = consts[0]
        state = {hh: s_scr[hh] for hh in range(hpb)}

        def unit(hh, j):
            sl = slice(hh * DH, (hh + 1) * DH)
            c = _chunk_in_block(j, rev)
            rows = slice(c * CHUNK, (c + 1) * CHUNK)
            b_ref = b_scr.at[hh * SCAN_CB + j]
            qc, kc, vc, gc = q_ref[rows, sl], k_ref[rows, sl], v_ref[rows, sl], g_ref[rows, sl]
            cum = _split_dot(tri, gc)
            b_ref[...] = cum
            yield
            terms = _chunk_terms(cum, b_ref, consts, rev)
            ops = [((qc * wq).astype(BF16), (kc * wk).astype(BF16)) for wq, wk in terms]
            tot = _colsum(gc)
            qe = (qc * jnp.exp(cum)).astype(BF16)
            ke = (kc * jnp.exp(tot - cum)).astype(BF16)
            vb = vc.astype(BF16)
            yield
            scs = [_dot(qt, kt, NT) for qt, kt in ops]
            kv = _dot(vb, ke, TN)
            yield
            a = jnp.zeros((CHUNK, CHUNK), F32)
            for sc, m in zip(scs, masks):
                a = a + jnp.where(m, sc, 0.0)
            o_intra = _dot(a.astype(BF16), vb, NN)
            yield
            st = state[hh]
            st_ref[hh, c] = st
            o_ref[rows, sl] = o_intra + _dot(qe, st.astype(BF16), NT)
            state[hh] = st * jnp.exp(tot) + kv
            yield

        _run_staged([unit(hh, j) for hh in range(hpb) for j in range(SCAN_CB)])
        for hh in range(hpb):
            s_scr[hh] = state[hh]

    ntb = t // SCAN_TB
    col = pl.BlockSpec((SCAN_TB, hpb * DH), lambda h, i: (_block_order(i, ntb, rev), h))
    return pl.pallas_call(
        body, name="scan_fwd_" + ("bw" if rev else "fw"), grid=(NH // hpb, ntb),
        in_specs=[col] * 4,
        out_specs=[col, pl.BlockSpec((hpb, SCAN_CB, DH, DH), lambda h, i: (h, _block_order(i, ntb, rev), 0, 0))],
        out_shape=[jax.ShapeDtypeStruct((t, D), F32), jax.ShapeDtypeStruct((NH, nc, DH, DH), F32)],
        scratch_shapes=[pltpu.VMEM((hpb, DH, DH), F32), pltpu.VMEM((hpb * SCAN_CB, CHUNK, DH), F32)],
        compiler_params=_cp("parallel", "arbitrary"),
    )(q, k, v, g)


def _scan_bwd(q, k, v, g, do, states, rev):
    t = q.shape[0]
    nc = t // CHUNK
    hpb = SCAN_HEADS_BWD

    def body(q_ref, k_ref, v_ref, g_ref, do_ref, st_ref, dq_ref, dk_ref, dv_ref, dg_ref, ds_scr, b_scr):
        consts = _scan_consts(rev)
        _, tri_t, lv, diag, diag_t = consts
        masks = [(pair, pair_t) for pair, pair_t, _ in lv] + [(diag, diag_t)]
        @pl.when(pl.program_id(1) == 0)
        def _():
            ds_scr[...] = jnp.zeros_like(ds_scr)

        tri = consts[0]
        dstate = {hh: ds_scr[hh] for hh in range(hpb)}

        def unit(hh, jj):
            sl = slice(hh * DH, (hh + 1) * DH)
            c = _chunk_in_block(SCAN_CB - 1 - jj, rev)
            rows = slice(c * CHUNK, (c + 1) * CHUNK)
            b_ref = b_scr.at[hh * SCAN_CB + jj]
            qc, kc, vc, gc = q_ref[rows, sl], k_ref[rows, sl], v_ref[rows, sl], g_ref[rows, sl]
            dob = do_ref[rows, sl].astype(BF16)
            vb = vc.astype(BF16)
            cum = _split_dot(tri, gc)
            b_ref[...] = cum
            da = _dot(dob, vb, NT)
            da_t = _dot(vb, dob, NT)
            yield
            terms = _chunk_terms(cum, b_ref, consts, rev)
            ops = [((qc * wq).astype(BF16), (kc * wk).astype(BF16)) for wq, wk in terms]
            tot = _colsum(gc)
            e_tot = jnp.exp(tot)
            e_b = jnp.exp(cum)
            e_t = jnp.exp(tot - cum)
            qeb = (qc * e_b).astype(BF16)
            keb = (kc * e_t).astype(BF16)
            dal = [(jnp.where(m, da, 0.0).astype(BF16), jnp.where(m_t, da_t, 0.0).astype(BF16)) for m, m_t in masks]
            yield
            ats = [_dot(ktb, qtb, NT) for qtb, ktb in ops]
            dqts = [_dot(d, ktb, NN) for (d, _), (_, ktb) in zip(dal, ops)]
            dkts = [_dot(d_t, qtb, NN) for (_, d_t), (qtb, _) in zip(dal, ops)]
            qd = _dot(dob, qeb, TN)
            yield
            a_t = jnp.zeros((CHUNK, CHUNK), F32)
            dq = jnp.zeros((CHUNK, DH), F32)
            dk = jnp.zeros((CHUNK, DH), F32)
            db = jnp.zeros((CHUNK, DH), F32)
            for at, dqt, dkt, (wq, wk), (qtb, ktb), (_, m_t) in zip(ats, dqts, dkts, terms, ops, masks):
                a_t = a_t + jnp.where(m_t, at, 0.0)
                dq = dq + dqt * wq
                dk = dk + dkt * wk
                db = db + dqt * qtb.astype(F32) - dkt * ktb.astype(F32)
            dv_intra = _dot(a_t.astype(BF16), dob, NN)
            st = st_ref[hh, c]
            stb = st.astype(BF16)
            dqe = _dot(dob, stb, NN)
            yield
            dst = dstate[hh]
            dstb = dst.astype(BF16)
            dstate[hh] = dst * e_tot + qd
            dv_ref[rows, sl] = (dv_intra + _dot(keb, dstb, NT)).astype(BF16)
            dke = _dot(vb, dstb, NN)
            yield
            qe = qeb.astype(F32)
            ke = keb.astype(F32)
            dq_ref[rows, sl] = (dq + dqe * e_b).astype(BF16)
            dk_ref[rows, sl] = (dk + dke * e_t).astype(BF16)
            db = db + dqe * qe - dke * ke
            dtot = _colsum(dstb.astype(F32) * stb.astype(F32)) * e_tot + _colsum(dke * ke)
            dg_ref[rows, sl] = _split_dot(tri_t, db) + dtot
            yield

        _run_staged([unit(hh, jj) for hh in range(hpb) for jj in range(SCAN_CB)])
        for hh in range(hpb):
            ds_scr[hh] = dstate[hh]

    ntb = t // SCAN_TB
    blk = lambda i: _block_order(ntb - 1 - i, ntb, rev)
    col = pl.BlockSpec((SCAN_TB, hpb * DH), lambda h, i: (blk(i), h))
    out = jax.ShapeDtypeStruct((t, D), F32)
    outb = jax.ShapeDtypeStruct((t, D), BF16)
    return pl.pallas_call(
        body, name="scan_bwd_" + ("bw" if rev else "fw"), grid=(NH // hpb, ntb),
        in_specs=[col] * 5 + [pl.BlockSpec((hpb, SCAN_CB, DH, DH), lambda h, i: (h, blk(i), 0, 0))],
        out_specs=[col] * 4,
        out_shape=[outb] * 3 + [out],
        scratch_shapes=[pltpu.VMEM((hpb, DH, DH), F32), pltpu.VMEM((hpb * SCAN_CB, CHUNK, DH), F32)],
        compiler_params=_cp("parallel", "arbitrary"),
    )(q, k, v, g, do, states)


W_IN_GRAD_CHUNKS = (("a", (0, 512)), ("b", (0, 128)), ("b", (128, 512)))
W_IN_REF = 6688


def _layout_w_in(w):
    return jnp.pad(w, ((0, 0), (0, W_IN_COLS - W_IN_REF)))


def _unlayout_w_in(d):
    return d[:, :W_IN_REF]


def _gate_cols(w):
    return jnp.pad(w, ((0, 0), (GOFF, GW - GOFF - D)))


def _gate_rows(w):
    return jnp.pad(w, ((GOFF, GW - GOFF - D), (0, 0)))


def _layout_wgk(w):
    r = w.shape[1]
    top = jnp.concatenate([w[0], jnp.zeros_like(w[0])], axis=1)
    bot = jnp.concatenate([jnp.zeros_like(w[1]), w[1]], axis=1)
    return jnp.concatenate([top, bot, jnp.zeros((DH - 2 * r, D), w.dtype)], axis=0)


def _unlayout_wgk(d, r=16):
    return jnp.stack([d[:r, :HW], d[r:2 * r, HW:]])


def _local_step(z, target, modc, modx, norms, onw, hg_lb, wgk, bgk, get_w_in, get_mix, get_ffn, send):
    n_pre1, n_post1, n_pre2, n_post2 = norms
    t = z.shape[0]
    tm = 1152 if t % 1152 == 0 else 256
    h1 = _prenorm(z, n_pre1, modc, modx, 0, 1, "prenorm1")
    w_in = get_w_in(h1)
    p = _matmul(h1, w_in, NN, BF16, "mm_in", t, 1024, D)
    q, v, k_f, k_b, g_f, g_b = _gates_fwd(p, hg_lb, wgk, bgk)
    o_f, st_f = _scan_fwd(q, k_f, v, g_f, False)
    o_b, st_b = _scan_fwd(q, k_b, v, g_b, True)
    y = _post_fwd(o_f, o_b, p, onw)
    w_br_hg, w_br_gla, w_out = get_mix(y)
    u1 = _matmul(y, w_br_hg, NN, BF16, "mm_br_hg", tm, GW, HW, a_off=0)
    u2 = _matmul(y, w_br_gla, NN, BF16, "mm_br_gla", tm, GW, HW, a_off=1)
    merged = _merge_fwd(p, u1, u2)
    y1 = _matmul(merged, w_out, NN, BF16, "mm_out", tm, 512, GW)
    z1, h2 = _mid_fwd(z, y1, n_post1, n_pre2, modc, modx)
    w_gu_t, w_down = get_ffn(h2)
    uv = _matmul(h2, w_gu_t, NT, BF16, "mm_gu", t, D_FF // 2, D)
    act = _swiglu_fwd(uv)
    y2 = _matmul(act, w_down, NN, BF16, "mm_down", t, 512, D_FF)
    dz, dy2, loss_vec, sm_final = _final(z1, y2, target, n_post2, modc, modx)
    dact = _matmul(dy2, w_down, NT, BF16, "mm_down_dx", t, D_FF // 2, D)
    d_w_down = _matmul(act, dy2, TN, BF16, "mm_down_dw", D_FF // 2, 1024, t)
    duv = _swiglu_bwd(uv, dact)
    dh2 = _matmul(duv, w_gu_t, NN, BF16, "mm_gu_dx", tm, 512, D_FF)
    d_w_gate_t = _matmul(duv, h2, TN, BF16, "mm_gate_dw", D_FF // 2, 1024, t, a_off=0, m_out=D_FF)
    d_w_up_t = _matmul(duv, h2, TN, BF16, "mm_up_dw", D_FF // 2, 1024, t, a_off=2, m_out=D_FF)
    dh2 = send(("w_down", "w_gate_t", "w_up_t"), (d_w_down, d_w_gate_t, d_w_up_t), dh2)
    dz, dy1, sm_mid = _mid_bwd(dh2, dz, z, z1, y1, n_post1, n_pre2, modc, modx)
    dmerged = _matmul(dy1, w_out, NT, BF16, "mm_out_dx", tm, GW, D)
    d_w_out = _matmul(merged, dy1, TN, BF16, "mm_out_dw", GW, 512, t)
    du1, du2, dgm = _merge_bwd(dmerged, p, u1, u2)
    dy_hg = _matmul(du1, w_br_hg, NT, BF16, "mm_br_hg_dx", tm, HW, GW)
    dy_gla = _matmul(du2, w_br_gla, NT, BF16, "mm_br_gla_dx", tm, HW, GW)
    d_w_br_hg = _matmul(y, du1, TN, BF16, "mm_br_hg_dw", HW, GW, t, a_off=0, m_out=HW)
    d_w_br_gla = _matmul(y, du2, TN, BF16, "mm_br_gla_dw", HW, GW, t, a_off=1, m_out=HW)
    dy_hg = send(("w_out", "w_br_hg", "w_br_gla"), (d_w_out, d_w_br_hg, d_w_br_gla), dy_hg)
    do, dgo, sm_post = _post_bwd(dy_hg, dy_gla, o_f, o_b, p, onw)
    dq_f, dk_f, dv_f, dg_f = _scan_bwd(q, k_f, v, g_f, do, st_f, False)
    dq_b, dk_b, dv_b, dg_b = _scan_bwd(q, k_b, v, g_b, do, st_b, True)
    dp, d_lb, d_wgk, d_bgk = _gates_bwd(p, hg_lb, wgk, bgk, dgm, dgo, dq_f, dq_b, dv_f, dv_b, dk_f, dk_b, dg_f, dg_b)
    d_w_in_a = _matmul(h1, dp, TN, BF16, "mm_in_dw_a", 512, 1024, t, a_off=0, m_out=D // 2)
    dp = send(("w_in_a",), (d_w_in_a,), dp)
    d_w_in_b = _matmul(h1, dp, TN, BF16, "mm_in_dw_b", 512, 1024, t, a_off=1, m_out=D // 2)
    dp = send(("w_in_b",), (d_w_in_b,), dp)
    dh1 = _matmul(dp, w_in, NT, BF16, "mm_in_dx", tm, 512, W_IN_COLS // 2)
    grad_x, sm_pre = _pre_bwd(dh1, dz, z, n_pre1, modc, modx)
    return dict(loss_vec=loss_vec, grad_x=grad_x, sm_final=sm_final, sm_mid=sm_mid, sm_post=sm_post, sm_pre=sm_pre,
                d_lb=d_lb, d_wgk=d_wgk, d_bgk=d_bgk)


MESH = pl.DeviceIdType.MESH
ANY = pl.BlockSpec(memory_space=pl.ANY)
N_REL = N_DEV - 1


def _place():
    return lax.axis_index("x"), lax.axis_index("y"), lax.axis_index("c")


def _slot(p):
    return 4 * p[0] + 2 * p[1] + p[2]


def _all_gather(arrays, name):
    n = len(arrays)

    def body(*refs):
        ins, outs = refs[:n], refs[n:2 * n]
        send_sems, recv_sems, local_sems = refs[2 * n:]
        x, y, c = _place()
        me, sibling = (x, y, c), (x, y, 1 - c)
        chips = [(1 - x, y), (x, 1 - y), (1 - x, 1 - y)]

        def copy(a, k, block, to, src=None):
            dst = outs[a].at[_slot(block)]
            return pltpu.make_async_remote_copy(
                src_ref=dst if src is None else src, dst_ref=dst,
                send_sem=send_sems.at[N_REL * a + k], recv_sem=recv_sems.at[N_REL * a + k],
                device_id=to, device_id_type=MESH)

        mine = [pltpu.make_async_copy(ins[a], outs[a].at[_slot(me)], local_sems.at[a]) for a in range(n)]
        for cp in mine:
            cp.start()
        first = []
        for a in range(n):
            first.append(copy(a, 0, me, sibling, src=ins[a]))
            first += [copy(a, 1 + j, me, (*chip, c), src=ins[a]) for j, chip in enumerate(chips)]
        for cp in first:
            cp.start()
        passed = []
        for j, chip in enumerate(chips):
            for a in range(n):
                copy(a, 1 + j, (*chip, c), me).wait_recv()
                fwd = copy(a, 4 + j, (*chip, c), sibling)
                fwd.start()
                passed.append(fwd)
        for a in range(n):
            copy(a, 0, sibling, me).wait_recv()
        for j, chip in enumerate(chips):
            for a in range(n):
                copy(a, 4 + j, (*chip, 1 - c), me).wait_recv()
        for cp in first + passed:
            cp.wait_send()
        for cp in mine:
            cp.wait()

    return pl.pallas_call(
        body, name=name,
        in_specs=[ANY] * n, out_specs=[ANY] * n,
        out_shape=[jax.ShapeDtypeStruct((N_DEV,) + a.shape, a.dtype) for a in arrays],
        scratch_shapes=[pltpu.SemaphoreType.DMA((N_REL * n,)), pltpu.SemaphoreType.DMA((N_REL * n,)),
                        pltpu.SemaphoreType.DMA((n,))],
    )(*arrays)


def _exchange(arrays, name):
    n = len(arrays)

    def body(*refs):
        ins, outs = refs[:n], refs[n:2 * n]
        send_sems, recv_sems, local_sems = refs[2 * n:]
        x, y, c = _place()
        me = _slot((x, y, c))
        mine = [pltpu.make_async_copy(ins[a].at[me], outs[a].at[me], local_sems.at[a]) for a in range(n)]
        for cp in mine:
            cp.start()
        copies = []
        for a in range(n):
            for k in range(1, N_DEV):
                flip = lambda v, bit: 1 - v if bit else v
                peer = (flip(x, k & 4), flip(y, k & 2), flip(c, k & 1))
                copies.append(pltpu.make_async_remote_copy(
                    src_ref=ins[a].at[_slot(peer)], dst_ref=outs[a].at[me],
                    send_sem=send_sems.at[N_REL * a + k - 1], recv_sem=recv_sems.at[N_REL * a + k - 1],
                    device_id=peer, device_id_type=MESH))
                copies[-1].start()
        i = 0
        for a in range(n):
            for k in range(1, N_DEV):
                flip = lambda v, bit: 1 - v if bit else v
                peer = (flip(x, k & 4), flip(y, k & 2), flip(c, k & 1))
                pltpu.make_async_remote_copy(
                    src_ref=ins[a].at[_slot(peer)], dst_ref=outs[a].at[_slot(peer)],
                    send_sem=send_sems.at[N_REL * a + k - 1], recv_sem=recv_sems.at[N_REL * a + k - 1],
                    device_id=peer, device_id_type=MESH).wait_recv()
                i += 1
        for cp in copies:
            cp.wait_send()
        for cp in mine:
            cp.wait()

    return pl.pallas_call(
        body, name=name,
        in_specs=[ANY] * n, out_specs=[ANY] * n,
        out_shape=[jax.ShapeDtypeStruct(a.shape, a.dtype) for a in arrays],
        scratch_shapes=[pltpu.SemaphoreType.DMA((N_REL * n,)), pltpu.SemaphoreType.DMA((N_REL * n,)),
                        pltpu.SemaphoreType.DMA((n,))],
    )(*arrays)


HBM = pl.BlockSpec(memory_space=pltpu.HBM)
SEM = pl.BlockSpec(memory_space=pltpu.SEMAPHORE)
EFFECT = pltpu.SideEffectType.DATAFLOW_SIDE_EFFECTING


def _peer_of(x, y, c, k):
    flip = lambda v, bit: 1 - v if bit else v
    return flip(x, k & 4), flip(y, k & 2), flip(c, k & 1)


def _view_whole(src, slot):
    return src


def _view_near(src, slot):
    return src


_view_near.peers = (1, 2, 4, 6)


def _view_block(src, slot):
    return src.at[slot]


W_IN_SHARD = W_IN_REF // N_DEV


def _view_window(rows):
    def view(src, slot):
        col0 = pl.multiple_of((W_IN_SHARD * slot // DH) * DH, DH)
        return src.at[pl.ds(rows[0], rows[1] - rows[0]), pl.ds(col0, D)]
    return view


def _split_copies(view, srcs, lands, send_sems, recv_sems, local_sems):
    x, y, c = _place()
    me = _slot((x, y, c))
    local, sends, waits = [], [], []
    for a, (src, land) in enumerate(zip(srcs, lands)):
        local.append(pltpu.make_async_copy(view(src, me), land.at[me], local_sems.at[a]))
        for k in getattr(view, "peers", range(1, N_DEV)):
            peer = _peer_of(x, y, c, k)
            mine = view(src, _slot(peer))
            sems = dict(send_sem=send_sems.at[N_REL * a + k - 1], recv_sem=recv_sems.at[N_REL * a + k - 1],
                        device_id=peer, device_id_type=MESH)
            sends.append(pltpu.make_async_remote_copy(src_ref=mine, dst_ref=land.at[me], **sems))
            waits.append(pltpu.make_async_remote_copy(src_ref=mine, dst_ref=land.at[_slot(peer)], **sems))
    return local, sends, waits


def _split_start(view, land_shapes, srcs, name, after):
    n = len(srcs)
    lands = [lax.empty(shp, s.dtype) for shp, s in zip(land_shapes, srcs)]

    def body(*refs):
        src_refs, land_refs = refs[:n], refs[n:2 * n]
        send_sems, recv_sems, local_sems = refs[2 * n + 1:2 * n + 4]
        token = refs[-1]
        local, sends, _ = _split_copies(view, src_refs, land_refs, send_sems, recv_sems, local_sems)
        for cp in local + sends:
            cp.start()
        token[...] = jnp.zeros_like(token)

    hbm = lambda a: pltpu.with_memory_space_constraint(a, pltpu.HBM)
    out = pl.pallas_call(
        body, name=name,
        out_shape=(pltpu.SemaphoreType.DMA((N_REL * n,)), pltpu.SemaphoreType.DMA((N_REL * n,)),
                   pltpu.SemaphoreType.DMA((n,)),
                   *[pltpu.HBM(s.shape, s.dtype) for s in srcs], *[pltpu.HBM(l.shape, l.dtype) for l in lands],
                   jax.ShapeDtypeStruct((8, DH), F32)),
        in_specs=[HBM] * (2 * n) + [ANY],
        out_specs=(SEM, SEM, SEM, *([HBM] * (2 * n)), pl.BlockSpec(memory_space=pltpu.VMEM)),
        input_output_aliases={i: 3 + i for i in range(2 * n)},
        compiler_params=pltpu.CompilerParams(has_side_effects=EFFECT),
    )(*[hbm(s) for s in srcs], *[hbm(l) for l in lands], after)
    handle = dict(view=view, n=n, sems=out[:3], srcs=list(out[3:3 + n]), lands=list(out[3 + n:3 + 2 * n]))
    return handle, out[-1]


def _split_wait(handle, name, after, srcs=None):
    view, n, sems, lands = handle["view"], handle["n"], handle["sems"], handle["lands"]
    srcs = handle["srcs"] if srcs is None else srcs
    afters = list(after) if isinstance(after, (list, tuple)) else [after]

    def body(*refs):
        src_refs, land_refs = refs[:n], refs[n:2 * n]
        send_sems, recv_sems, local_sems = refs[2 * n:2 * n + 3]
        local, _, waits = _split_copies(view, src_refs, land_refs, send_sems, recv_sems, local_sems)
        for cp in waits:
            cp.wait_send()
            cp.wait_recv()
        for cp in local:
            cp.wait()

    out = pl.pallas_call(
        body, name=name,
        out_shape=(*[pltpu.HBM(s.shape, s.dtype) for s in srcs], *[pltpu.HBM(l.shape, l.dtype) for l in lands]),
        in_specs=[HBM] * (2 * n) + [SEM, SEM, SEM] + [ANY] * len(afters),
        out_specs=tuple([HBM] * (2 * n)),
        input_output_aliases={i: i for i in range(2 * n)},
        compiler_params=pltpu.CompilerParams(has_side_effects=EFFECT),
    )(*srcs, *lands, *sems, *afters)
    handle["srcs"] = list(out[:n])
    return list(out[n:])


def _tie(x, token, name):
    def body(x_ref, t_ref, o_ref):
        pass

    return pl.pallas_call(
        body, name=name, out_shape=jax.ShapeDtypeStruct(x.shape, x.dtype),
        in_specs=[ANY, ANY], out_specs=ANY, input_output_aliases={0: 0},
    )(x, token)


def _forward_to_sibling(land, name):
    def body(land_ref, out_ref, send_sems, recv_sems):
        x, y, c = _place()
        sibling = (x, y, 1 - c)
        chips = [(1 - x, y), (x, 1 - y), (1 - x, 1 - y)]

        def copy(j, core):
            blk = _slot((*chips[j], core))
            return pltpu.make_async_remote_copy(src_ref=land_ref.at[blk], dst_ref=out_ref.at[blk],
                                                send_sem=send_sems.at[j], recv_sem=recv_sems.at[j],
                                                device_id=sibling, device_id_type=MESH)

        sends = [copy(j, c) for j in range(3)]
        for cp in sends:
            cp.start()
        for j in range(3):
            copy(j, 1 - c).wait_recv()
        for cp in sends:
            cp.wait_send()

    return pl.pallas_call(
        body, name=name, in_specs=[ANY], out_specs=ANY, input_output_aliases={0: 0},
        out_shape=jax.ShapeDtypeStruct(land.shape, land.dtype),
        scratch_shapes=[pltpu.SemaphoreType.DMA((3,)), pltpu.SemaphoreType.DMA((3,))],
    )(land)


def _mod_fwd(a, w, b):
    def body(a_ref, w_ref, b_ref, o_ref):
        o_ref[...] = _dot(_silu(a_ref[...]), w_ref[...], NN, precision=HI) + b_ref[...]

    return pl.pallas_call(
        body, name="mod_fwd", out_shape=jax.ShapeDtypeStruct((a.shape[0], w.shape[1]), F32),
        compiler_params=pltpu.CompilerParams(vmem_limit_bytes=VMEM_LIMIT),
    )(a, w, b)


def _mod_bwd(a, d, w):
    def body(a_ref, d_ref, w_ref, dw_ref, dc_ref):
        av = a_ref[...]
        dv = d_ref[...]
        dw_ref[...] = _dot(_silu(av), dv, TN, precision=HI)
        da = _dot(dv[0:8, :], w_ref[...], NT, precision=HI) * _dsilu(av[0:8, :])
        row = lax.broadcasted_iota(jnp.int32, da.shape, 0)
        dc_ref[...] = jnp.where(row == 0, da, 0.0)

    return pl.pallas_call(
        body, name="mod_bwd",
        out_shape=[jax.ShapeDtypeStruct(w.shape, F32), jax.ShapeDtypeStruct((8, w.shape[0]), F32)],
        compiler_params=pltpu.CompilerParams(vmem_limit_bytes=VMEM_LIMIT),
    )(a, d, w)


def _sum_devices(g):
    def body(g_ref, o_ref):
        acc = g_ref[0]
        for i in range(1, g.shape[0]):
            acc = acc + g_ref[i]
        o_ref[...] = acc

    return pl.pallas_call(body, name="sum_devices_%d" % g.shape[1],
                          out_shape=jax.ShapeDtypeStruct(g.shape[1:], F32))(g)


def _sum_windows(g, name):
    n, r, c = g.shape
    tr = 128

    def body(g_ref, o_ref):
        x, y, cc = _place()
        lane0 = (W_IN_SHARD * _slot((x, y, cc))) % DH
        acc = g_ref[0].astype(F32)
        for i in range(1, n):
            acc = acc + g_ref[i].astype(F32)
        o_ref[...] = pltpu.roll(acc, (c - lane0) % c, 1).T

    return pl.pallas_call(
        body, name=name, grid=(r // tr,),
        in_specs=[pl.BlockSpec((n, tr, c), lambda i: (0, i, 0))],
        out_specs=pl.BlockSpec((c, tr), lambda i: (0, i)),
        out_shape=jax.ShapeDtypeStruct((c, r), F32),
        compiler_params=_cp("parallel"),
    )(g)


def _adam_rows(r, c, n):
    budget = 6 * 1024 * 1024
    best = None
    for tr in range(16, r + 1, 16):
        if r % tr == 0 and tr * c * (2 * n + 28) <= budget:
            best = tr
    return best if best is not None else r


def _adamw(g, w, m, v, name):
    n, r, c = g.shape
    tr = _adam_rows(r, c, n)
    bc1 = 1.0 - ADAM_B1 ** ADAM_STEP
    bc2 = 1.0 - ADAM_B2 ** ADAM_STEP

    def body(g_ref, w_ref, m_ref, v_ref, go_ref, d_ref, mo_ref, vo_ref):
        grad = g_ref[0].astype(F32)
        for i in range(1, n):
            grad = grad + g_ref[i].astype(F32)
        go_ref[...] = grad
        m_new = ADAM_B1 * m_ref[...] + (1.0 - ADAM_B1) * grad
        v_new = ADAM_B2 * v_ref[...] + (1.0 - ADAM_B2) * (grad * grad)
        mo_ref[...] = m_new
        vo_ref[...] = v_new
        d_ref[...] = -ADAM_LR * ((m_new / bc1) / (jnp.sqrt(v_new / bc2) + ADAM_EPS) + ADAM_WD * w_ref[...])

    blk = pl.BlockSpec((tr, c), lambda i: (i, 0))
    out = jax.ShapeDtypeStruct((r, c), F32)
    return pl.pallas_call(
        body, name=name, grid=(r // tr,),
        in_specs=[pl.BlockSpec((n, tr, c), lambda i: (0, i, 0)), blk, blk, blk],
        out_specs=[blk] * 4, out_shape=[out] * 4,
        compiler_params=_cp("parallel"),
    )(g, w, m, v)


def kernel(x, c, ctx, c_ctx, w_mod, b_mod, norm_pre1, norm_post1, norm_pre2, norm_post2, w_in, hg_lb, hg_onorm, gla_w_gk, gla_b_gk, gla_onorm, w_br_hg, w_br_gla, w_out, w_ff_gate, w_ff_up, w_ff_down, loss_target, m_c_ctx, m_w_mod, m_b_mod, m_norm_pre1, m_norm_post1, m_norm_pre2, m_norm_post2, m_w_in, m_hg_lb, m_hg_onorm, m_gla_w_gk, m_gla_b_gk, m_gla_onorm, m_w_br_hg, m_w_br_gla, m_w_out, m_w_ff_gate, m_w_ff_up, m_w_ff_down, v_c_ctx, v_w_mod, v_b_mod, v_norm_pre1, v_norm_post1, v_norm_pre2, v_norm_post2, v_w_in, v_hg_lb, v_hg_onorm, v_gla_w_gk, v_gla_b_gk, v_gla_onorm, v_w_br_hg, v_w_br_gla, v_w_out, v_w_ff_gate, v_w_ff_up, v_w_ff_down):
    xi, yi, ci = lax.axis_index("x"), lax.axis_index("y"), lax.axis_index("c")
    me = 4 * xi + 2 * yi + ci
    t = CTX + x.shape[1]

    c_all, lb_g, wgk_g, bgk_g = _all_gather([c, hg_lb, gla_w_gk[0], gla_b_gk[0]], "ag_small")
    tr_ = lambda a: jnp.swapaxes(a[0], 0, 1)
    big = [w_in[0], w_br_hg[0], w_br_gla[0], w_out[0], tr_(w_ff_gate), tr_(w_ff_up), w_ff_down[0]]
    big_bf = [w.astype(BF16) for w in big]
    cols = lambda g: jnp.transpose(g, (1, 0, 2)).reshape(g.shape[1], N_DEV * g.shape[2])

    def get_w_in(after):
        land, = _split_wait(w_in_handle, "ag_w_in_wait", after)
        return _layout_w_in(cols(_forward_to_sibling(land, "ag_w_in_forward")))

    def get_mix(after):
        g_brh, g_brg, g_out = _split_wait(mix_handle, "ag_mix_wait", after)
        return _gate_cols(cols(g_brh)), _gate_cols(cols(g_brg)), _gate_rows(g_out.reshape(D, D))

    def get_ffn(after):
        g_gate, g_up, g_down = _split_wait(ffn_handle, "ag_ffn_wait", after)
        return (g_gate.reshape(D_FF, D), g_up.reshape(D_FF, D)), g_down.reshape(D_FF, D)

    hg_lb_full = jnp.transpose(lb_g, (1, 2, 0, 3)).reshape(2, 2, HW)
    wgk_k = _layout_wgk(jnp.transpose(wgk_g, (1, 2, 0, 3)).reshape(2, 16, HW)).astype(BF16)
    bgk_k = jnp.transpose(bgk_g, (1, 0, 2)).reshape(1, D)
    onw = jnp.concatenate([jnp.tile(hg_onorm, (1, NH // 2)), jnp.tile(gla_onorm, (1, NH // 2))], axis=1)

    n_mod = w_mod.shape[2]
    a9 = jnp.concatenate([c_ctx[None], c_all[:, 0], jnp.zeros((16 - 1 - N_DEV, D), F32)], axis=0)
    b_loc = lax.dynamic_slice(b_mod, (0, me * n_mod), (1, n_mod))
    s_loc = _mod_fwd(a9, w_mod[0], b_loc)
    s_all, = _all_gather([s_loc], "ag_mod")
    mod_all = jnp.transpose(s_all, (1, 0, 2)).reshape(16, N_DEV * n_mod)
    pad8 = lambda m: jnp.concatenate([m.reshape(6, D), jnp.zeros((2, D), F32)], axis=0)
    modc = pad8(mod_all[0])
    modx = pad8(lax.dynamic_slice(mod_all, (1 + me, 0), (1, N_DEV * n_mod))[0])

    gathered = lambda arrs: [(N_DEV,) + a.shape for a in arrs]
    w_in_handle, tok = _split_start(_view_near, gathered(big_bf[:1]), big_bf[:1], "ag_w_in_start", s_all)
    mix_handle, tok = _split_start(_view_whole, gathered(big_bf[1:4]), big_bf[1:4], "ag_mix_start", tok)
    ffn_handle, tok = _split_start(_view_whole, gathered(big_bf[4:]), big_bf[4:], "ag_ffn_start", tok)

    z = _tie(jnp.concatenate([ctx[0], x[0]], axis=0), tok, "tie_z")
    norms = (norm_pre1, norm_post1, norm_pre2, norm_post2)
    shard = lambda d: jnp.transpose(d.reshape(d.shape[0], N_DEV, -1), (1, 0, 2)).astype(BF16)
    rowshard = lambda d: d.reshape(N_DEV, d.shape[0] // N_DEV, d.shape[1]).astype(BF16)
    sent, w_in_grad = [], {}

    def send_w_in(i, x_after):
        half, rows = W_IN_GRAD_CHUNKS[i]
        handle, tok = _split_start(_view_window(rows), [(N_DEV, rows[1] - rows[0], D)], w_in_grad[half],
                                   "grads_w_in%d_start" % i, x_after)
        w_in_grad[half] = handle["srcs"]
        sent.append(("w_in%d" % i, ["w_in#%d" % i], handle))
        return _tie(x_after, tok, "tie_w_in%d" % i)

    def send(names, grads, x_after):
        if names == ("w_in_a",):
            w_in_grad["a"] = list(grads)
            return send_w_in(0, x_after)
        if names == ("w_in_b",):
            w_in_grad["b"] = list(grads)
            return x_after
        arrs, leaves = [], []
        for nm, g in zip(names, grads):
            if nm in ("w_gate_t", "w_up_t"):
                arrs.append(rowshard(g))
                leaves.append({"w_gate_t": "w_ff_gate", "w_up_t": "w_ff_up"}[nm])
            elif nm == "w_down":
                arrs.append(rowshard(g))
                leaves.append("w_ff_down")
            elif nm == "w_out":
                arrs.append(rowshard(g[GOFF:GOFF + D]))
                leaves.append(nm)
            else:
                arrs.append(shard(g[:, GOFF:GOFF + D]))
                leaves.append(nm)
        handle, tok = _split_start(_view_block, [a.shape for a in arrs], arrs, "grads_%s_start" % names[0], x_after)
        sent.append((names[0], leaves, handle))
        return _tie(x_after, tok, "tie_" + names[0])

    r = _local_step(z, loss_target[0], modc, modx, norms, onw, hg_lb_full, wgk_k, bgk_k,
                    get_w_in, get_mix, get_ffn, send)
    grad_x = r["grad_x"][None]

    sm_pre, sm_mid, sm_fin = r["sm_pre"], r["sm_mid"], r["sm_final"]
    dmodc = jnp.stack([sm_pre[0], sm_pre[2], sm_mid[4], sm_mid[0], sm_mid[2], sm_fin[0]]).reshape(-1)
    dmodx = jnp.stack([sm_pre[1], sm_pre[3], sm_mid[5], sm_mid[1], sm_mid[3], sm_fin[1]]).reshape(-1)
    on = r["sm_post"][0].reshape(NH, DH)
    pieces = [dmodc, dmodx, sm_pre[4], sm_mid[7], sm_mid[6], sm_fin[2], on[:NH // 2].sum(0), on[NH // 2:].sum(0),
              r["d_lb"][:2].reshape(-1), _unlayout_wgk(r["d_wgk"]).reshape(-1), r["d_bgk"][0]]
    loss_local = (0.5 / D) * jnp.sum(r["loss_vec"])
    pieces.append(jnp.concatenate([loss_local.reshape(1), jnp.zeros((DH - 1,), F32)]))
    sizes = [p.shape[0] for p in pieces]
    pack = jnp.concatenate(pieces).reshape(-1, DH)
    pack_all, = _all_gather([pack], "ag_small_grads")
    pack_all = send_w_in(1, pack_all)
    tot = _sum_devices(pack_all).reshape(-1)
    offs = [sum(sizes[:i]) for i in range(len(sizes))]
    part = lambda i: tot[offs[i]:offs[i] + sizes[i]]
    dmodc_t, dmodx_t = part(0), part(1)
    g_b_mod = (dmodc_t + dmodx_t)[None]
    g_norms = [part(i)[None] for i in (2, 3, 4, 5)]
    g_hg_on, g_gla_on = part(6)[None], part(7)[None]
    lb0 = lax.dynamic_slice(part(8).reshape(2, HW), (0, me * (HW // N_DEV)), (2, HW // N_DEV))
    g_hg_lb = jnp.stack([lb0, -lb0])
    g_wgk = lax.dynamic_slice(part(9).reshape(2, 16, HW), (0, 0, me * (HW // N_DEV)), (2, 16, HW // N_DEV))[None]
    g_bgk = lax.dynamic_slice(part(10).reshape(2, HW), (0, me * (HW // N_DEV)), (2, HW // N_DEV))[None]
    loss = part(11)[0]

    dmx_all = pack_all.reshape(N_DEV, -1)[:, sizes[0]:sizes[0] + sizes[1]]
    d9 = jnp.concatenate([lax.dynamic_slice(dmodc_t[None], (0, me * n_mod), (1, n_mod)),
                          lax.dynamic_slice(dmx_all, (0, me * n_mod), (N_DEV, n_mod)),
                          jnp.zeros((16 - 1 - N_DEV, n_mod), F32)], axis=0)
    g_w_mod, dcc_part = _mod_bwd(a9, d9, w_mod[0])
    dcc_all, = _all_gather([dcc_part], "ag_c_ctx")
    dcc_all = send_w_in(2, dcc_all)
    g_c_ctx = _sum_devices(dcc_all)[0]

    recv = {}
    for first, leaves, handle in sent:
        if not first.startswith("w_in"):
            recv.update(zip(leaves, _split_wait(handle, "grads_%s_wait" % first, g_c_ctx)))
    moms = [(m_w_in, v_w_in), (m_w_br_hg, v_w_br_hg), (m_w_br_gla, v_w_br_gla), (m_w_out, v_w_out),
            (m_w_ff_gate, v_w_ff_gate), (m_w_ff_up, v_w_ff_up), (m_w_ff_down, v_w_ff_down)]
    names = ["w_in", "w_br_hg", "w_br_gla", "w_out", "w_ff_gate", "w_ff_up", "w_ff_down"]
    res = {}

    def update(nm, w, m, v):
        if nm in ("w_ff_gate", "w_ff_up"):
            outs = _adamw(recv[nm], w, tr_(m), tr_(v), "adamw_" + nm)
            res[nm] = [jnp.swapaxes(o, 0, 1)[None] for o in outs]
        else:
            res[nm] = [o[None] for o in _adamw(recv[nm], w, m[0], v[0], "adamw_" + nm)]

    for nm, w, (m, v) in list(zip(names, big, moms))[1:]:
        update(nm, w, m, v)
    res["w_mod"] = [o[None] for o in _adamw(g_w_mod[None], w_mod[0], m_w_mod[0], v_w_mod[0], "adamw_w_mod")]

    small = [("c_ctx", c_ctx, m_c_ctx, v_c_ctx, g_c_ctx), ("b_mod", b_mod, m_b_mod, v_b_mod, g_b_mod),
             ("norm_pre1", norm_pre1, m_norm_pre1, v_norm_pre1, g_norms[0]),
             ("norm_post1", norm_post1, m_norm_post1, v_norm_post1, g_norms[1]),
             ("norm_pre2", norm_pre2, m_norm_pre2, v_norm_pre2, g_norms[2]),
             ("norm_post2", norm_post2, m_norm_post2, v_norm_post2, g_norms[3]),
             ("hg_lb", hg_lb, m_hg_lb, v_hg_lb, g_hg_lb), ("hg_onorm", hg_onorm, m_hg_onorm, v_hg_onorm, g_hg_on),
             ("gla_w_gk", gla_w_gk, m_gla_w_gk, v_gla_w_gk, g_wgk), ("gla_b_gk", gla_b_gk, m_gla_b_gk, v_gla_b_gk, g_bgk),
             ("gla_onorm", gla_onorm, m_gla_onorm, v_gla_onorm, g_gla_on)]
    flat = lambda k: jnp.concatenate([s[k].reshape(-1) for s in small]).reshape(-1, DH)
    outs = _adamw(flat(4)[None], flat(1), flat(2), flat(3), "adamw_small")
    off = 0
    for nm, w, _, _, _ in small:
        res[nm] = [o.reshape(-1)[off:off + w.size].reshape(w.shape) for o in outs]
        off += w.size

    done = [res[nm][0] for nm in names[1:]] + [res["w_mod"][0], outs[0]]
    sums = []
    for i, (first, leaves, handle) in enumerate(s for s in sent if s[0].startswith("w_in")):
        half = W_IN_GRAD_CHUNKS[i][0]
        land, = _split_wait(handle, "grads_%s_wait" % first, done, srcs=w_in_grad[half])
        w_in_grad[half] = handle["srcs"]
        sums.append(_sum_windows(land, "sum_windows%d" % i))
    g_t = jnp.concatenate(sums, axis=1)[:W_IN_SHARD]
    lin = lambda a: a.reshape(W_IN_SHARD * D // DH, DH)
    major = lambda a: lin(jnp.transpose(a, (2, 0, 1)))
    outs = _adamw(lin(g_t)[None], major(w_in), major(m_w_in), major(v_w_in), "adamw_w_in")
    res["w_in"] = [jnp.transpose(o.reshape(W_IN_SHARD, 1, D), (1, 2, 0)) for o in outs]

    order = ["c_ctx", "w_mod", "b_mod", "norm_pre1", "norm_post1", "norm_pre2", "norm_post2", "w_in", "hg_lb",
             "hg_onorm", "gla_w_gk", "gla_b_gk", "gla_onorm", "w_br_hg", "w_br_gla", "w_out", "w_ff_gate", "w_ff_up",
             "w_ff_down"]
    return (loss, grad_x, *[res[n][k] for k in range(4) for n in order])
```

```python
import functools

import jax
import jax.numpy as jnp
from jax import lax
from jax.experimental import pallas as pl
from jax.experimental.pallas import tpu as pltpu

F32 = jnp.float32
BF16 = jnp.bfloat16
HI = lax.Precision.HIGHEST

N_DEV = 8
D = 1024
CTX = 256
HW = 512
DH = 128
NH = 8
D_FF = 2816
EPS = 1e-6
GLA_NORM = 16.0
CHUNK = 64
TR = 256
NCT = CTX // TR
W_IN_COLS = 7168
MAIN0 = 0
LR0 = 4608
GW = 1152
GOFF = 32
GATE_HG0 = LR0
GATE_GLA0 = LR0 + D
LEVELS = (32, 16, 8)
EXP_CLAMP = 80.0
VMEM_LIMIT = 48 * 1024 * 1024

ADAM_LR, ADAM_B1, ADAM_B2, ADAM_EPS, ADAM_WD, ADAM_STEP = 0.001, 0.9, 0.999, 1e-08, 0.01, 10


def _cp(*sem):
    return pltpu.CompilerParams(dimension_semantics=sem, vmem_limit_bytes=VMEM_LIMIT)


def _sig(x):
    return jax.nn.sigmoid(x)


def _silu(x):
    return x * _sig(x)


def _dsilu(x):
    s = _sig(x)
    return s * (1.0 + x * (1.0 - s))


def _rstd(x):
    return lax.rsqrt(jnp.mean(x * x, axis=-1, keepdims=True) + EPS)


def _rms_bwd(a, y, r):
    return r * (a - y * (r * r) * jnp.mean(a * y, axis=-1, keepdims=True))


def _colsum(x):
    return jnp.sum(x, axis=0, keepdims=True)


def _dot(a, b, dims, precision=None):
    return lax.dot_general(a, b, (dims, ((), ())), preferred_element_type=F32, precision=precision)


NN = ((1,), (0,))
NT = ((1,), (1,))
TN = ((0,), (0,))

SCAN_HEADS_FWD = 4
SCAN_HEADS_BWD = 4


def _split_dot(m, x):
    mb = m.astype(BF16)
    x1 = x.astype(BF16)
    r1 = x - x1.astype(F32)
    x2 = r1.astype(BF16)
    x3 = (r1 - x2.astype(F32)).astype(BF16)
    return _dot(mb, x1, NN) + _dot(mb, x2, NN) + _dot(mb, x3, NN)


def _matmul(a, b, dims, out_dtype, name, tm, tn, tk, a_off=0, m_out=None):
    pair = isinstance(b, (tuple, list))
    bs = list(b) if pair else [b]
    b1 = bs[0]
    rows = b1.shape[0] * len(bs)
    half = None
    if dims == NN:
        m, k, n = a.shape[0], rows, b1.shape[1]
        a_spec = pl.BlockSpec((tm, tk), lambda i, j, kk: (i, kk + a_off))
        half = b1.shape[0] // tk
        b_maps = [lambda i, j, kk: (kk, j)] if not pair else [
            lambda i, j, kk: (jnp.minimum(kk, half - 1), j), lambda i, j, kk: (jnp.maximum(kk - half, 0), j)]
        b_specs = [pl.BlockSpec((tk, tn), f) for f in b_maps]
        axis = 2
    elif dims == NT:
        m, k, n = a.shape[0], b1.shape[1], rows
        a_spec = pl.BlockSpec((tm, tk), lambda i, j, kk: (i, kk + a_off))
        half = b1.shape[0] // tn
        b_maps = [lambda i, j, kk: (j, kk)] if not pair else [
            lambda i, j, kk: (jnp.minimum(j, half - 1), kk), lambda i, j, kk: (jnp.maximum(j - half, 0), kk)]
        b_specs = [pl.BlockSpec((tn, tk), f) for f in b_maps]
        axis = 1
    else:
        assert not pair
        m, k = (a.shape[1] if m_out is None else m_out), a.shape[0]
        n = b1.shape[1]
        a_spec = pl.BlockSpec((tk, tm), lambda i, j, kk: (kk, i + a_off))
        b_specs = [pl.BlockSpec((tk, tn), lambda i, j, kk: (kk, j))]
    assert m % tm == 0 and n % tn == 0 and k % tk == 0, (name, m, n, k, tm, tn, tk)
    nk = k // tk
    nb = len(bs)

    def body(a_ref, *refs):
        o_ref = refs[nb]
        if pair:
            bv = jnp.where(pl.program_id(axis) < half, refs[0][...], refs[1][...])
        else:
            bv = refs[0][...]
        part = _dot(a_ref[...], bv, dims)
        if nk == 1:
            o_ref[...] = part.astype(o_ref.dtype)
            return
        acc_ref = refs[nb + 1]
        kk = pl.program_id(2)

        @pl.when(kk == 0)
        def _():
            acc_ref[...] = part

        @pl.when(kk > 0)
        def _():
            acc_ref[...] += part

        @pl.when(kk == nk - 1)
        def _():
            o_ref[...] = acc_ref[...].astype(o_ref.dtype)

    return pl.pallas_call(
        body,
        name=name,
        grid=(m // tm, n // tn, nk),
        in_specs=[a_spec] + b_specs,
        out_specs=pl.BlockSpec((tm, tn), lambda i, j, kk: (i, j)),
        out_shape=jax.ShapeDtypeStruct((m, n), out_dtype),
        scratch_shapes=[] if nk == 1 else [pltpu.VMEM((tm, tn), F32)],
        compiler_params=_cp("parallel", "parallel", "arbitrary"),
    )(a, *bs)


def _row(c):
    return pl.BlockSpec((TR, c), lambda i: (i, 0))


def _rowcol(width, cb):
    return pl.BlockSpec((TR, width), lambda i: (i, cb))


def _full(shape):
    return pl.BlockSpec(shape, lambda i: (0,) * len(shape))


def _mod_row(mc_ref, mx_ref, k, is_ctx):
    return jnp.where(is_ctx, mc_ref[k:k + 1, :], mx_ref[k:k + 1, :])


def _acc_row(ref, k, val):
    ref[k:k + 1, :] += val


def _acc_mod(ref, k, is_ctx, val):
    zero = jnp.zeros_like(val)
    ref[k:k + 1, :] += jnp.where(is_ctx, val, zero)
    ref[k + 1:k + 2, :] += jnp.where(is_ctx, zero, val)


def _prenorm(z, nw, modc, modx, i_shift, i_scale, name):
    t = z.shape[0]

    def body(z_ref, nw_ref, mc_ref, mx_ref, h_ref):
        is_ctx = pl.program_id(0) < NCT
        x = z_ref[...]
        n = x * _rstd(x) * nw_ref[...]
        h = n * (1.0 + _mod_row(mc_ref, mx_ref, i_scale, is_ctx)) + _mod_row(mc_ref, mx_ref, i_shift, is_ctx)
        h_ref[...] = h.astype(BF16)

    return pl.pallas_call(
        body, name=name, grid=(t // TR,),
        in_specs=[_row(D), _full((1, D)), _full((8, D)), _full((8, D))],
        out_specs=_row(D),
        out_shape=jax.ShapeDtypeStruct((t, D), BF16),
        compiler_params=_cp("parallel"),
    )(z, nw, modc, modx)


def _hg_lb(lb_ref, d):
    a0 = lb_ref[0, d:d + 1, :]
    a1 = lb_ref[1, d:d + 1, :]
    mx = jnp.maximum(a0, a1)
    e0 = jnp.exp(a0 - mx)
    e1 = jnp.exp(a1 - mx)
    return e0 / (e0 + e1)


def _log_sigmoid(x):
    return jnp.minimum(x, 0.0) - jnp.log(1.0 + jnp.exp(-jnp.abs(x)))


def _gates_fwd(p, hg_lb, wgk, bgk):
    t = p.shape[0]
    seg = lambda j: _rowcol(HW, MAIN0 // HW + j)

    def body(hq_ref, hi_ref, hf_ref, hb_ref, gq_ref, gk_ref, gv_ref, lr_ref, lb_ref, wgk_ref, bgk_ref,
             q_ref, v_ref, kf_ref, kb_ref, gf_ref, gb_ref):
        q_ref[:, :HW] = _silu(hq_ref[...].astype(F32)).astype(BF16)
        q_ref[:, HW:] = (gq_ref[...].astype(F32) * (DH ** -0.5)).astype(BF16)
        v_ref[:, :HW] = hi_ref[...]
        v_ref[:, HW:] = gv_ref[...]
        xg = _dot(lr_ref[...].astype(BF16), wgk_ref[...], NN) + bgk_ref[...]
        for d, (raw_ref, k_ref, g_ref) in enumerate(((hf_ref, kf_ref, gf_ref), (hb_ref, kb_ref, gb_ref))):
            lbd = _hg_lb(lb_ref, d)
            f = lbd + (1.0 - lbd) * _sig(raw_ref[...].astype(F32))
            k_ref[:, :HW] = (1.0 - f).astype(BF16)
            k_ref[:, HW:] = gk_ref[...]
            g_ref[:, :HW] = jnp.log(f)
            g_ref[:, HW:] = _log_sigmoid(xg[:, d * HW:(d + 1) * HW]) * (1.0 / GLA_NORM)

    out = jax.ShapeDtypeStruct((t, D), F32)
    outb = jax.ShapeDtypeStruct((t, D), BF16)
    return pl.pallas_call(
        body, name="gates_fwd", grid=(t // TR,),
        in_specs=[seg(0), seg(1), seg(2), seg(3), seg(5), seg(6), seg(7), _rowcol(DH, LR0 // DH),
                  _full((2, 2, HW)), _full((DH, D)), _full((1, D))],
        out_specs=[_row(D)] * 6,
        out_shape=[outb] * 4 + [out] * 2,
        compiler_params=_cp("parallel"),
    )(p, p, p, p, p, p, p, p, hg_lb, wgk, bgk)


def _post_fwd(o_fw, o_bw, p, onw):
    t = o_fw.shape[0]

    def body(of_ref, ob_ref, g1_ref, g2_ref, w_ref, y_ref):
        for h in range(NH):
            sl = slice(h * DH, (h + 1) * DH)
            o = of_ref[:, sl] + ob_ref[:, sl]
            g_ref = g1_ref if h < NH // 2 else g2_ref
            gs = slice((h % (NH // 2)) * DH, (h % (NH // 2) + 1) * DH)
            n = o * _rstd(o) * w_ref[:, sl]
            y_ref[:, sl] = (n * _silu(g_ref[:, gs].astype(F32))).astype(BF16)

    return pl.pallas_call(
        body, name="post_fwd", grid=(t // TR,),
        in_specs=[_row(D), _row(D), _rowcol(HW, MAIN0 // HW + 4), _rowcol(HW, MAIN0 // HW + 8), _full((1, D))],
        out_specs=_row(D),
        out_shape=jax.ShapeDtypeStruct((t, D), BF16),
        compiler_params=_cp("parallel"),
    )(o_fw, o_bw, p, p, onw)


def _gate_window_specs(col0):
    return [_rowcol(HW, col0 // HW), _rowcol(HW, col0 // HW + 1), _rowcol(DH, (col0 + 2 * HW) // DH)]


def _gate_window(refs):
    return jnp.concatenate([r[...].astype(F32) for r in refs], axis=1)


def _merge_fwd(p, u1, u2):
    t = p.shape[0]

    def body(a0, a1, a2, b0, b1, b2, u1_ref, u2_ref, m_ref):
        f = lambda r: r[...].astype(F32)
        m_ref[...] = (_sig(_gate_window((a0, a1, a2))) * f(u1_ref)
                      + _sig(_gate_window((b0, b1, b2))) * f(u2_ref)).astype(BF16)

    return pl.pallas_call(
        body, name="merge_fwd", grid=(t // TR,),
        in_specs=_gate_window_specs(GATE_HG0) + _gate_window_specs(GATE_GLA0) + [_row(GW), _row(GW)],
        out_specs=_row(GW),
        out_shape=jax.ShapeDtypeStruct((t, GW), BF16),
        compiler_params=_cp("parallel"),
    )(p, p, p, p, p, p, u1, u2)


def _mid_fwd(z, y1, nw_post, nw_pre, modc, modx):
    t = z.shape[0]

    def body(z_ref, y_ref, wpo_ref, wpr_ref, mc_ref, mx_ref, z1_ref, h_ref):
        is_ctx = pl.program_id(0) < NCT
        y = y_ref[...].astype(F32)
        z1 = z_ref[...] + _mod_row(mc_ref, mx_ref, 2, is_ctx) * (y * _rstd(y) * wpo_ref[...])
        z1_ref[...] = z1
        n = z1 * _rstd(z1) * wpr_ref[...]
        h = n * (1.0 + _mod_row(mc_ref, mx_ref, 4, is_ctx)) + _mod_row(mc_ref, mx_ref, 3, is_ctx)
        h_ref[...] = h.astype(BF16)

    return pl.pallas_call(
        body, name="mid_fwd", grid=(t // TR,),
        in_specs=[_row(D), _row(D), _full((1, D)), _full((1, D)), _full((8, D)), _full((8, D))],
        out_specs=[_row(D), _row(D)],
        out_shape=[jax.ShapeDtypeStruct((t, D), F32), jax.ShapeDtypeStruct((t, D), BF16)],
        compiler_params=_cp("parallel"),
    )(z, y1, nw_post, nw_pre, modc, modx)


def _swiglu_fwd(uv):
    t = uv.shape[0]

    def body(u_ref, v_ref, a_ref):
        a_ref[...] = (_silu(u_ref[...].astype(F32)) * v_ref[...].astype(F32)).astype(BF16)

    return pl.pallas_call(
        body, name="swiglu_fwd", grid=(t // TR,),
        in_specs=[_rowcol(D_FF, 0), _rowcol(D_FF, 1)],
        out_specs=_row(D_FF),
        out_shape=jax.ShapeDtypeStruct((t, D_FF), BF16),
        compiler_params=_cp("parallel"),
    )(uv, uv)


def _swiglu_bwd(uv, da):
    t = uv.shape[0]

    def body(u_ref, v_ref, da_ref, d_ref):
        u = u_ref[...].astype(F32)
        d = da_ref[...].astype(F32)
        d_ref[:, :D_FF] = (d * v_ref[...].astype(F32) * _dsilu(u)).astype(BF16)
        d_ref[:, D_FF:] = (d * _silu(u)).astype(BF16)

    return pl.pallas_call(
        body, name="swiglu_bwd", grid=(t // TR,),
        in_specs=[_rowcol(D_FF, 0), _rowcol(D_FF, 1), _row(D_FF)],
        out_specs=_row(2 * D_FF),
        out_shape=jax.ShapeDtypeStruct((t, 2 * D_FF), BF16),
        compiler_params=_cp("parallel"),
    )(uv, uv, da)


def _final(z1, y2, target, nw, modc, modx):
    t = z1.shape[0]

    def body(z1_ref, y_ref, tg_ref, w_ref, mc_ref, mx_ref, dz_ref, dy_ref, loss_ref, sm_ref):
        i = pl.program_id(0)
        is_ctx = i < NCT

        @pl.when(i == 0)
        def _():
            loss_ref[...] = jnp.zeros_like(loss_ref)
            sm_ref[...] = jnp.zeros_like(sm_ref)

        g = _mod_row(mc_ref, mx_ref, 5, is_ctx)
        y = y_ref[...].astype(F32)
        r = _rstd(y)
        w = w_ref[...]
        yr = y * r
        n = yr * w
        e = z1_ref[...] + g * n - tg_ref[...]
        lat = jnp.where(is_ctx, 0.0, 1.0)
        loss_ref[...] += lat * _colsum(e * e)
        dz = e * (lat / D)
        dz_ref[...] = dz
        _acc_mod(sm_ref, 0, is_ctx, _colsum(dz * n))
        dn = dz * g
        _acc_row(sm_ref, 2, _colsum(dn * yr))
        dy_ref[...] = _rms_bwd(dn * w, y, r).astype(BF16)

    return pl.pallas_call(
        body, name="final", grid=(t // TR,),
        in_specs=[_row(D), _row(D), pl.BlockSpec((TR, D), lambda i: (jnp.maximum(i - NCT, 0), 0)),
                  _full((1, D)), _full((8, D)), _full((8, D))],
        out_specs=[_row(D), _row(D), _full((1, D)), _full((8, D))],
        out_shape=[jax.ShapeDtypeStruct((t, D), F32), jax.ShapeDtypeStruct((t, D), BF16),
                   jax.ShapeDtypeStruct((1, D), F32), jax.ShapeDtypeStruct((8, D), F32)],
        compiler_params=_cp("arbitrary"),
    )(z1, y2, target, nw, modc, modx)


def _mid_bwd(dh2, dz, z, z1, y1, nw_post, nw_pre, modc, modx):
    t = z.shape[0]

    def body(dh_ref, dz_ref, z_ref, z1_ref, y_ref, wpo_ref, wpr_ref, mc_ref, mx_ref, dzo_ref, dy_ref, sm_ref):
        i = pl.program_id(0)
        is_ctx = i < NCT

        @pl.when(i == 0)
        def _():
            sm_ref[...] = jnp.zeros_like(sm_ref)

        dh = dh_ref[...].astype(F32)
        z1 = z1_ref[...]
        r = _rstd(z1)
        zr = z1 * r
        wpr = wpr_ref[...]
        n = zr * wpr
        _acc_mod(sm_ref, 0, is_ctx, _colsum(dh))
        _acc_mod(sm_ref, 2, is_ctx, _colsum(dh * n))
        dn = dh * (1.0 + _mod_row(mc_ref, mx_ref, 4, is_ctx))
        _acc_row(sm_ref, 6, _colsum(dn * zr))
        dz1 = dz_ref[...] + _rms_bwd(dn * wpr, z1, r)
        dzo_ref[...] = dz1
        y = y_ref[...].astype(F32)
        r1 = _rstd(y)
        yr = y * r1
        wpo = wpo_ref[...]
        g = _mod_row(mc_ref, mx_ref, 2, is_ctx)
        _acc_mod(sm_ref, 4, is_ctx, _colsum(dz1 * (yr * wpo)))
        dn1 = dz1 * g
        _acc_row(sm_ref, 7, _colsum(dn1 * yr))
        dy_ref[...] = _rms_bwd(dn1 * wpo, y, r1).astype(BF16)

    return pl.pallas_call(
        body, name="mid_bwd", grid=(t // TR,),
        in_specs=[_row(D)] * 5 + [_full((1, D)), _full((1, D)), _full((8, D)), _full((8, D))],
        out_specs=[_row(D), _row(D), _full((8, D))],
        out_shape=[jax.ShapeDtypeStruct((t, D), F32), jax.ShapeDtypeStruct((t, D), BF16),
                   jax.ShapeDtypeStruct((8, D), F32)],
        compiler_params=_cp("arbitrary"),
    )(dh2, dz, z, z1, y1, nw_post, nw_pre, modc, modx)


def _pre_bwd(dh1, dz, z, nw, modc, modx):
    t = z.shape[0]

    def body(dh_ref, dz_ref, z_ref, w_ref, mc_ref, mx_ref, dzo_ref, sm_ref):
        i = pl.program_id(0)
        is_ctx = i < NCT

        @pl.when(i == 0)
        def _():
            sm_ref[...] = jnp.zeros_like(sm_ref)

        dh = dh_ref[...].astype(F32)
        x = z_ref[...]
        r = _rstd(x)
        xr = x * r
        w = w_ref[...]
        _acc_mod(sm_ref, 0, is_ctx, _colsum(dh))
        _acc_mod(sm_ref, 2, is_ctx, _colsum(dh * (xr * w)))
        dn = dh * (1.0 + _mod_row(mc_ref, mx_ref, 1, is_ctx))
        _acc_row(sm_ref, 4, _colsum(dn * xr))
        dzo_ref[...] = dz_ref[...] + _rms_bwd(dn * w, x, r)

    return pl.pallas_call(
        body, name="pre_bwd", grid=(t // TR,),
        in_specs=[_row(D)] * 3 + [_full((1, D)), _full((8, D)), _full((8, D))],
        out_specs=[pl.BlockSpec((TR, D), lambda i: (jnp.maximum(i - NCT, 0), 0)), _full((8, D))],
        out_shape=[jax.ShapeDtypeStruct((t - CTX, D), F32), jax.ShapeDtypeStruct((8, D), F32)],
        compiler_params=_cp("arbitrary"),
    )(dh1, dz, z, nw, modc, modx)


def _merge_bwd(dm, p, u1, u2):
    t = dm.shape[0]

    def body(dm_ref, a0, a1, a2, b0, b1, b2, u1_ref, u2_ref, du1_ref, du2_ref, dg_ref):
        dm_ = dm_ref[...].astype(F32)
        s1 = _sig(_gate_window((a0, a1, a2)))
        s2 = _sig(_gate_window((b0, b1, b2)))
        du1_ref[...] = (dm_ * s1).astype(BF16)
        du2_ref[...] = (dm_ * s2).astype(BF16)
        dg_ref[:, :GW] = (dm_ * u1_ref[...].astype(F32) * s1 * (1.0 - s1)).astype(BF16)
        dg_ref[:, GW:] = (dm_ * u2_ref[...].astype(F32) * s2 * (1.0 - s2)).astype(BF16)

    return pl.pallas_call(
        body, name="merge_bwd", grid=(t // TR,),
        in_specs=[_row(GW)] + _gate_window_specs(GATE_HG0) + _gate_window_specs(GATE_GLA0) + [_row(GW), _row(GW)],
        out_specs=[_row(GW), _row(GW), _row(2 * GW)],
        out_shape=[jax.ShapeDtypeStruct((t, GW), BF16), jax.ShapeDtypeStruct((t, GW), BF16),
                   jax.ShapeDtypeStruct((t, 2 * GW), BF16)],
        compiler_params=_cp("parallel"),
    )(dm, p, p, p, p, p, p, u1, u2)


def _post_bwd(dy_hg, dy_gla, o_fw, o_bw, p, onw):
    t = o_fw.shape[0]

    def body(d1_ref, d2_ref, of_ref, ob_ref, g1_ref, g2_ref, w_ref, do_ref, dg_ref, sm_ref):
        @pl.when(pl.program_id(0) == 0)
        def _():
            sm_ref[...] = jnp.zeros_like(sm_ref)

        for h in range(NH):
            sl = slice(h * DH, (h + 1) * DH)
            gs = slice((h % (NH // 2)) * DH, (h % (NH // 2) + 1) * DH)
            g_ref, d_ref = (g1_ref, d1_ref) if h < NH // 2 else (g2_ref, d2_ref)
            o = of_ref[:, sl] + ob_ref[:, sl]
            r = _rstd(o)
            orr = o * r
            w = w_ref[:, sl]
            gt = g_ref[:, gs].astype(F32)
            dy = d_ref[:, gs].astype(F32)
            dg_ref[:, sl] = (dy * (orr * w) * _dsilu(gt)).astype(BF16)
            dn = dy * _silu(gt)
            sm_ref[0:1, sl] += _colsum(dn * orr)
            do_ref[:, sl] = _rms_bwd(dn * w, o, r)

    return pl.pallas_call(
        body, name="post_bwd", grid=(t // TR,),
        in_specs=[_row(HW), _row(HW), _row(D), _row(D), _rowcol(HW, MAIN0 // HW + 4), _rowcol(HW, MAIN0 // HW + 8),
                  _full((1, D))],
        out_specs=[_row(D), _row(D), _full((8, D))],
        out_shape=[jax.ShapeDtypeStruct((t, D), F32), jax.ShapeDtypeStruct((t, D), BF16),
                   jax.ShapeDtypeStruct((8, D), F32)],
        compiler_params=_cp("arbitrary"),
    )(dy_hg, dy_gla, o_fw, o_bw, p, p, onw)


def _gates_bwd(p, hg_lb, wgk, bgk, dgm, dgo, dq_f, dq_b, dv_f, dv_b, dk_f, dk_b, dg_f, dg_b):
    t = p.shape[0]
    seg = lambda j: _rowcol(HW, MAIN0 // HW + j)

    def body(hq_ref, hf_ref, hb_ref, lr_ref, lb_ref, wgk_ref, bgk_ref, dgm_ref, dgo_ref,
             dqf_ref, dqb_ref, dvf_ref, dvb_ref, dkf_ref, dkb_ref, dgf_ref, dgb_ref,
             dp_ref, dlb_ref, dw_ref, db_ref):
        @pl.when(pl.program_id(0) == 0)
        def _():
            dlb_ref[...] = jnp.zeros_like(dlb_ref)
            dw_ref[...] = jnp.zeros_like(dw_ref)
            db_ref[...] = jnp.zeros_like(db_ref)

        c0 = MAIN0

        def put(j, val):
            dp_ref[:, c0 + j * HW:c0 + (j + 1) * HW] = val.astype(BF16)

        dq = dqf_ref[...].astype(F32) + dqb_ref[...].astype(F32)
        dv = dvf_ref[...].astype(F32) + dvb_ref[...].astype(F32)
        put(0, dq[:, :HW] * _dsilu(hq_ref[...].astype(F32)))
        put(1, dv[:, :HW])
        put(5, dq[:, HW:] * (DH ** -0.5))
        put(7, dv[:, HW:])
        put(6, dkf_ref[:, HW:].astype(F32) + dkb_ref[:, HW:].astype(F32))
        dp_ref[:, c0 + 4 * HW:c0 + 5 * HW] = dgo_ref[:, :HW]
        dp_ref[:, c0 + 8 * HW:c0 + 9 * HW] = dgo_ref[:, HW:]
        lr = lr_ref[...].astype(BF16)
        xg = _dot(lr, wgk_ref[...], NN) + bgk_ref[...]
        dxg = []
        for d, (raw_ref, dk_ref, dg_ref) in enumerate(((hf_ref, dkf_ref, dgf_ref), (hb_ref, dkb_ref, dgb_ref))):
            lbd = _hg_lb(lb_ref, d)
            s = _sig(raw_ref[...].astype(F32))
            f = lbd + (1.0 - lbd) * s
            df = dg_ref[:, :HW] / f - dk_ref[:, :HW].astype(F32)
            put(2 + d, df * (1.0 - lbd) * s * (1.0 - s))
            dlb_ref[d:d + 1, :] += _colsum(df * (1.0 - s)) * (lbd * (1.0 - lbd))
            dxg.append(dg_ref[:, HW:] * (1.0 / GLA_NORM) * _sig(-xg[:, d * HW:(d + 1) * HW]))
        dxg = jnp.concatenate(dxg, axis=1)
        db_ref[0:1, :] += _colsum(dxg)
        dxg_b = dxg.astype(BF16)
        dw_ref[...] += _dot(lr, dxg_b, TN)
        dlr = _dot(dxg_b, wgk_ref[...], NT)
        dp_ref[:, LR0:LR0 + DH] = (dlr + dgm_ref[:, :DH].astype(F32)).astype(BF16)
        dp_ref[:, LR0 + DH:GATE_GLA0] = dgm_ref[:, DH:D]
        dp_ref[:, GATE_GLA0:GATE_GLA0 + DH] = dgm_ref[:, D:GW] + dgm_ref[:, GW:GW + DH]
        dp_ref[:, GATE_GLA0 + DH:GATE_GLA0 + GW] = dgm_ref[:, GW + DH:]
        dp_ref[:, GATE_GLA0 + GW:] = jnp.zeros((TR, W_IN_COLS - GATE_GLA0 - GW), BF16)

    return pl.pallas_call(
        body, name="gates_bwd", grid=(t // TR,),
        in_specs=[seg(0), seg(2), seg(3), _rowcol(DH, LR0 // DH), _full((2, 2, HW)), _full((DH, D)), _full((1, D)),
                  _row(2 * GW), _row(D)] + [_row(D)] * 8,
        out_specs=[_row(W_IN_COLS), _full((8, HW)), _full((DH, D)), _full((8, D))],
        out_shape=[jax.ShapeDtypeStruct((t, W_IN_COLS), BF16), jax.ShapeDtypeStruct((8, HW), F32),
                   jax.ShapeDtypeStruct((DH, D), F32), jax.ShapeDtypeStruct((8, D), F32)],
        compiler_params=_cp("arbitrary"),
    )(p, p, p, p, hg_lb, wgk, bgk, dgm, dgo, dq_f, dq_b, dv_f, dv_b, dk_f, dk_b, dg_f, dg_b)


def _scan_consts(rev):
    r = lax.broadcasted_iota(jnp.int32, (CHUNK, CHUNK), 0)
    u = lax.broadcasted_iota(jnp.int32, (CHUNK, CHUNK), 1)
    rp = lax.broadcasted_iota(jnp.int32, (CHUNK, 1), 0)
    if rev:
        r, u, rp = CHUNK - 1 - r, CHUNK - 1 - u, CHUNK - 1 - rp
    tri = jnp.where(u <= r, 1.0, 0.0).astype(F32)
    tri_t = jnp.where(r <= u, 1.0, 0.0).astype(F32)
    lv = []
    for b in LEVELS:
        sh = b.bit_length() - 1
        pair = ((r >> sh) == (u >> sh) + 1) & (((u >> sh) & 1) == 0)
        pair_t = ((u >> sh) == (r >> sh) + 1) & (((r >> sh) & 1) == 0)
        tside = ((rp >> sh) & 1) == 1
        lv.append((pair, pair_t, tside, jnp.where(tside, 1.0, -1.0).astype(F32)))
    bd = LEVELS[-1].bit_length() - 1
    diag = ((r >> bd) == (u >> bd)) & (u <= r)
    diag_t = ((r >> bd) == (u >> bd)) & (r <= u)
    return tri, tri_t, lv, diag, diag_t


def _row_of(pos, rev):
    return CHUNK - 1 - pos if rev else pos


def _chunk_terms(cum, b_scr, consts, rev):
    _, _, lv, _, _ = consts
    terms = []
    for b, (_, _, _, sgn) in zip(LEVELS, lv):
        pieces = []
        for j in range(CHUNK // (2 * b)):
            row = _row_of(2 * b * j + b - 1, rev)
            pieces.append(jnp.broadcast_to(b_scr[row:row + 1, :], (2 * b, DH)))
        if rev:
            pieces = pieces[::-1]
        bnd = pieces[0] if len(pieces) == 1 else jnp.concatenate(pieces, axis=0)
        terms.append(jnp.exp((cum - bnd) * sgn))
    b = LEVELS[-1]
    pieces = []
    for j in range(CHUNK // b):
        if j == 0:
            pieces.append(jnp.zeros((b, DH), F32))
        else:
            row = _row_of(b * j - 1, rev)
            pieces.append(jnp.broadcast_to(b_scr[row:row + 1, :], (b, DH)))
    if rev:
        pieces = pieces[::-1]
    start = jnp.concatenate(pieces, axis=0)
    wq = jnp.exp(jnp.minimum(cum - start, 0.0))
    wk = jnp.exp(jnp.minimum(start - cum, EXP_CLAMP))
    terms.append((wq, wk))
    return terms


def _run_staged(units):
    live = list(units)
    while live:
        nxt = []
        for u in live:
            try:
                next(u)
                nxt.append(u)
            except StopIteration:
                pass
        live = nxt


SCAN_TB = 256
SCAN_CB = SCAN_TB // CHUNK


def _block_order(i, ntb, rev):
    nctx = CTX // SCAN_TB
    if not rev:
        return i
    return jnp.where(i < nctx, nctx - 1 - i, ntb - 1 - (i - nctx))


def _chunk_in_block(j, rev):
    return SCAN_CB - 1 - j if rev else j


def _scan_fwd(q, k, v, g, rev):
    t = q.shape[0]
    nc = t // CHUNK
    hpb = SCAN_HEADS_FWD

    def body(q_ref, k_ref, v_ref, g_ref, o_ref, st_ref, s_scr, b_scr):
        consts = _scan_consts(rev)
        _, _, lv, diag, _ = consts
        masks = [lvl[0] for lvl in lv] + [diag]

        @pl.when(pl.program_id(1) == 0)
        def _():
            s_scr[...] = jnp.zeros_like(s_scr)

        tri = consts[0]
        state = {hh: s_scr[hh] for hh in range(hpb)}

        def unit(hh, j):
            sl = slice(hh * DH, (hh + 1) * DH)
            c = _chunk_in_block(j, rev)
            rows = slice(c * CHUNK, (c + 1) * CHUNK)
            b_ref = b_scr.at[hh * SCAN_CB + j]
            qc, kc, vc, gc = q_ref[rows, sl], k_ref[rows, sl], v_ref[rows, sl], g_ref[rows, sl]
            cum = _split_dot(tri, gc)
            b_ref[...] = cum
            yield
            terms = _chunk_terms(cum, b_ref, consts, rev)
            qf, kf = qc.astype(F32), kc.astype(F32)
            xs = [(jnp.where(tside, qf, kf) * w).astype(BF16) for w, (_, _, tside, _) in zip(terms[:-1], lv)]
            qd, kd = (qf * terms[-1][0]).astype(BF16), (kf * terms[-1][1]).astype(BF16)
            tot = _colsum(gc)
            qe = (qf * jnp.exp(cum)).astype(BF16)
            ke = (kf * jnp.exp(tot - cum)).astype(BF16)
            vb = vc.astype(BF16)
            yield
            scs = [_dot(x, x, NT) for x in xs] + [_dot(qd, kd, NT)]
            kv = _dot(vb, ke, TN)
            yield
            a = jnp.zeros((CHUNK, CHUNK), F32)
            for sc, m in zip(scs, masks):
                a = a + jnp.where(m, sc, 0.0)
            o_intra = _dot(a.astype(BF16), vb, NN)
            yield
            st = state[hh]
            st_ref[hh, c] = st
            o_ref[rows, sl] = o_intra + _dot(qe, st.astype(BF16), NT)
            state[hh] = st * jnp.exp(tot) + kv
            yield

        _run_staged([unit(hh, j) for hh in range(hpb) for j in range(SCAN_CB)])
        for hh in range(hpb):
            s_scr[hh] = state[hh]

    ntb = t // SCAN_TB
    col = pl.BlockSpec((SCAN_TB, hpb * DH), lambda h, i: (_block_order(i, ntb, rev), h))
    return pl.pallas_call(
        body, name="scan_fwd_" + ("bw" if rev else "fw"), grid=(NH // hpb, ntb),
        in_specs=[col] * 4,
        out_specs=[col, pl.BlockSpec((hpb, SCAN_CB, DH, DH), lambda h, i: (h, _block_order(i, ntb, rev), 0, 0))],
        out_shape=[jax.ShapeDtypeStruct((t, D), F32), jax.ShapeDtypeStruct((NH, nc, DH, DH), F32)],
        scratch_shapes=[pltpu.VMEM((hpb, DH, DH), F32), pltpu.VMEM((hpb * SCAN_CB, CHUNK, DH), F32)],
        compiler_params=_cp("parallel", "arbitrary"),
    )(q, k, v, g)


def _scan_bwd(q, k, v, g, do, states, rev):
    t = q.shape[0]
    nc = t // CHUNK
    hpb = SCAN_HEADS_BWD

    def body(q_ref, k_ref, v_ref, g_ref, do_ref, st_ref, dq_ref, dk_ref, dv_ref, dg_ref, ds_scr, b_scr):
        consts = _scan_consts(rev)
        _, tri_t, lv, diag, diag_t = consts
        masks = [(lvl[0], lvl[1]) for lvl in lv] + [(diag, diag_t)]
        @pl.when(pl.program_id(1) == 0)
        def _():
            ds_scr[...] = jnp.zeros_like(ds_scr)

        tri = consts[0]
        dstate = {hh: ds_scr[hh] for hh in range(hpb)}

        def unit(hh, jj):
            sl = slice(hh * DH, (hh + 1) * DH)
            c = _chunk_in_block(SCAN_CB - 1 - jj, rev)
            rows = slice(c * CHUNK, (c + 1) * CHUNK)
            b_ref = b_scr.at[hh * SCAN_CB + jj]
            qc, kc, vc, gc = q_ref[rows, sl], k_ref[rows, sl], v_ref[rows, sl], g_ref[rows, sl]
            dob = do_ref[rows, sl].astype(BF16)
            vb = vc.astype(BF16)
            cum = _split_dot(tri, gc)
            b_ref[...] = cum
            da = _dot(dob, vb, NT)
            da_t = _dot(vb, dob, NT)
            yield
            terms = _chunk_terms(cum, b_ref, consts, rev)
            qf, kf = qc.astype(F32), kc.astype(F32)
            xs = [(jnp.where(tside, qf, kf) * w).astype(BF16) for w, (_, _, tside, _) in zip(terms[:-1], lv)]
            wqd, wkd = terms[-1]
            qdb, kdb = (qf * wqd).astype(BF16), (kf * wkd).astype(BF16)
            tot = _colsum(gc)
            e_tot = jnp.exp(tot)
            e_b = jnp.exp(cum)
            e_t = jnp.exp(tot - cum)
            qeb = (qf * e_b).astype(BF16)
            keb = (kf * e_t).astype(BF16)
            dsym = [(jnp.where(m, da, 0.0) + jnp.where(m_t, da_t, 0.0)).astype(BF16) for m, m_t in masks[:-1]]
            dad = (jnp.where(diag, da, 0.0).astype(BF16), jnp.where(diag_t, da_t, 0.0).astype(BF16))
            yield
            sym = [_dot(x, x, NT) for x in xs]
            dxs = [_dot(d, x, NN) for d, x in zip(dsym, xs)]
            at_d = _dot(kdb, qdb, NT)
            dqt_d = _dot(dad[0], kdb, NN)
            dkt_d = _dot(dad[1], qdb, NN)
            qd = _dot(dob, qeb, TN)
            yield
            a_t = jnp.where(diag_t, at_d, 0.0)
            dq = dqt_d * wqd
            dk = dkt_d * wkd
            db = dqt_d * qdb.astype(F32) - dkt_d * kdb.astype(F32)
            for s, dx, x, w, (_, m_t, tside, sgn) in zip(sym, dxs, xs, terms[:-1], lv):
                a_t = a_t + jnp.where(m_t, s, 0.0)
                dxw = dx * w
                dq = dq + jnp.where(tside, dxw, 0.0)
                dk = dk + jnp.where(tside, 0.0, dxw)
                db = db + (dx * x.astype(F32)) * sgn
            dv_intra = _dot(a_t.astype(BF16), dob, NN)
            st = st_ref[hh, c]
            stb = st.astype(BF16)
            dqe = _dot(dob, stb, NN)
            yield
            dst = dstate[hh]
            dstb = dst.astype(BF16)
            dstate[hh] = dst * e_tot + qd
            dv_ref[rows, sl] = (dv_intra + _dot(keb, dstb, NT)).astype(BF16)
            dke = _dot(vb, dstb, NN)
            yield
            qe = qeb.astype(F32)
            ke = keb.astype(F32)
            dq_ref[rows, sl] = (dq + dqe * e_b).astype(BF16)
            dk_ref[rows, sl] = (dk + dke * e_t).astype(BF16)
            db = db + dqe * qe - dke * ke
            dtot = _colsum(dstb.astype(F32) * stb.astype(F32)) * e_tot + _colsum(dke * ke)
            dg_ref[rows, sl] = _split_dot(tri_t, db) + dtot
            yield

        _run_staged([unit(hh, jj) for hh in range(hpb) for jj in range(SCAN_CB)])
        for hh in range(hpb):
            ds_scr[hh] = dstate[hh]

    ntb = t // SCAN_TB
    blk = lambda i: _block_order(ntb - 1 - i, ntb, rev)
    col = pl.BlockSpec((SCAN_TB, hpb * DH), lambda h, i: (blk(i), h))
    out = jax.ShapeDtypeStruct((t, D), F32)
    outb = jax.ShapeDtypeStruct((t, D), BF16)
    return pl.pallas_call(
        body, name="scan_bwd_" + ("bw" if rev else "fw"), grid=(NH // hpb, ntb),
        in_specs=[col] * 5 + [pl.BlockSpec((hpb, SCAN_CB, DH, DH), lambda h, i: (h, blk(i), 0, 0))],
        out_specs=[col] * 4,
        out_shape=[outb] * 3 + [out],
        scratch_shapes=[pltpu.VMEM((hpb, DH, DH), F32), pltpu.VMEM((hpb * SCAN_CB, CHUNK, DH), F32)],
        compiler_params=_cp("parallel", "arbitrary"),
    )(q, k, v, g, do, states)


W_IN_GRAD_CHUNKS = (("a", (0, 512)), ("b", (0, 128)), ("b", (128, 512)))
W_IN_REF = 6688


def _layout_w_in(w):
    return jnp.pad(w, ((0, 0), (0, W_IN_COLS - W_IN_REF)))


def _unlayout_w_in(d):
    return d[:, :W_IN_REF]


def _gate_cols(w):
    return jnp.pad(w, ((0, 0), (GOFF, GW - GOFF - D)))


def _gate_rows(w):
    return jnp.pad(w, ((GOFF, GW - GOFF - D), (0, 0)))


def _layout_wgk(w):
    r = w.shape[1]
    top = jnp.concatenate([w[0], jnp.zeros_like(w[0])], axis=1)
    bot = jnp.concatenate([jnp.zeros_like(w[1]), w[1]], axis=1)
    return jnp.concatenate([top, bot, jnp.zeros((DH - 2 * r, D), w.dtype)], axis=0)


def _unlayout_wgk(d, r=16):
    return jnp.stack([d[:r, :HW], d[r:2 * r, HW:]])


def _local_step(z, target, modc, modx, norms, onw, hg_lb, wgk, bgk, get_w_in, get_mix, get_ffn, send):
    n_pre1, n_post1, n_pre2, n_post2 = norms
    t = z.shape[0]
    tm = 1152 if t % 1152 == 0 else 256
    h1 = _prenorm(z, n_pre1, modc, modx, 0, 1, "prenorm1")
    w_in = get_w_in(h1)
    p = _matmul(h1, w_in, NN, BF16, "mm_in", t, 1024, D)
    q, v, k_f, k_b, g_f, g_b = _gates_fwd(p, hg_lb, wgk, bgk)
    o_f, st_f = _scan_fwd(q, k_f, v, g_f, False)
    o_b, st_b = _scan_fwd(q, k_b, v, g_b, True)
    y = _post_fwd(o_f, o_b, p, onw)
    w_br_hg, w_br_gla, w_out = get_mix(y)
    u1 = _matmul(y, w_br_hg, NN, BF16, "mm_br_hg", tm, GW, HW, a_off=0)
    u2 = _matmul(y, w_br_gla, NN, BF16, "mm_br_gla", tm, GW, HW, a_off=1)
    merged = _merge_fwd(p, u1, u2)
    y1 = _matmul(merged, w_out, NN, BF16, "mm_out", tm, 512, GW)
    z1, h2 = _mid_fwd(z, y1, n_post1, n_pre2, modc, modx)
    w_gu_t, w_down = get_ffn(h2)
    uv = _matmul(h2, w_gu_t, NT, BF16, "mm_gu", t, D_FF // 2, D)
    act = _swiglu_fwd(uv)
    y2 = _matmul(act, w_down, NN, BF16, "mm_down", t, 512, D_FF)
    dz, dy2, loss_vec, sm_final = _final(z1, y2, target, n_post2, modc, modx)
    dact = _matmul(dy2, w_down, NT, BF16, "mm_down_dx", t, D_FF // 2, D)
    d_w_down = _matmul(act, dy2, TN, BF16, "mm_down_dw", D_FF // 2, 1024, t)
    duv = _swiglu_bwd(uv, dact)
    dh2 = _matmul(duv, w_gu_t, NN, BF16, "mm_gu_dx", tm, 512, D_FF)
    d_w_gate_t = _matmul(duv, h2, TN, BF16, "mm_gate_dw", D_FF // 2, 1024, t, a_off=0, m_out=D_FF)
    d_w_up_t = _matmul(duv, h2, TN, BF16, "mm_up_dw", D_FF // 2, 1024, t, a_off=2, m_out=D_FF)
    dh2 = send(("w_down", "w_gate_t", "w_up_t"), (d_w_down, d_w_gate_t, d_w_up_t), dh2)
    dz, dy1, sm_mid = _mid_bwd(dh2, dz, z, z1, y1, n_post1, n_pre2, modc, modx)
    dmerged = _matmul(dy1, w_out, NT, BF16, "mm_out_dx", tm, GW, D)
    d_w_out = _matmul(merged, dy1, TN, BF16, "mm_out_dw", GW, 512, t)
    du1, du2, dgm = _merge_bwd(dmerged, p, u1, u2)
    dy_hg = _matmul(du1, w_br_hg, NT, BF16, "mm_br_hg_dx", tm, HW, GW)
    dy_gla = _matmul(du2, w_br_gla, NT, BF16, "mm_br_gla_dx", tm, HW, GW)
    d_w_br_hg = _matmul(y, du1, TN, BF16, "mm_br_hg_dw", HW, GW, t, a_off=0, m_out=HW)
    d_w_br_gla = _matmul(y, du2, TN, BF16, "mm_br_gla_dw", HW, GW, t, a_off=1, m_out=HW)
    dy_hg = send(("w_out", "w_br_hg", "w_br_gla"), (d_w_out, d_w_br_hg, d_w_br_gla), dy_hg)
    do, dgo, sm_post = _post_bwd(dy_hg, dy_gla, o_f, o_b, p, onw)
    dq_f, dk_f, dv_f, dg_f = _scan_bwd(q, k_f, v, g_f, do, st_f, False)
    dq_b, dk_b, dv_b, dg_b = _scan_bwd(q, k_b, v, g_b, do, st_b, True)
    dp, d_lb, d_wgk, d_bgk = _gates_bwd(p, hg_lb, wgk, bgk, dgm, dgo, dq_f, dq_b, dv_f, dv_b, dk_f, dk_b, dg_f, dg_b)
    d_w_in_a = _matmul(h1, dp, TN, BF16, "mm_in_dw_a", 512, 1024, t, a_off=0, m_out=D // 2)
    dp = send(("w_in_a",), (d_w_in_a,), dp)
    d_w_in_b = _matmul(h1, dp, TN, BF16, "mm_in_dw_b", 512, 1024, t, a_off=1, m_out=D // 2)
    dp = send(("w_in_b",), (d_w_in_b,), dp)
    dh1 = _matmul(dp, w_in, NT, BF16, "mm_in_dx", tm, 512, W_IN_COLS // 2)
    grad_x, sm_pre = _pre_bwd(dh1, dz, z, n_pre1, modc, modx)
    return dict(loss_vec=loss_vec, grad_x=grad_x, sm_final=sm_final, sm_mid=sm_mid, sm_post=sm_post, sm_pre=sm_pre,
                d_lb=d_lb, d_wgk=d_wgk, d_bgk=d_bgk)


MESH = pl.DeviceIdType.MESH
ANY = pl.BlockSpec(memory_space=pl.ANY)
N_REL = N_DEV - 1


def _place():
    return lax.axis_index("x"), lax.axis_index("y"), lax.axis_index("c")


def _slot(p):
    return 4 * p[0] + 2 * p[1] + p[2]


def _all_gather(arrays, name):
    n = len(arrays)

    def body(*refs):
        ins, outs = refs[:n], refs[n:2 * n]
        send_sems, recv_sems, local_sems = refs[2 * n:]
        x, y, c = _place()
        me, sibling = (x, y, c), (x, y, 1 - c)
        chips = [(1 - x, y), (x, 1 - y), (1 - x, 1 - y)]

        def copy(a, k, block, to, src=None):
            dst = outs[a].at[_slot(block)]
            return pltpu.make_async_remote_copy(
                src_ref=dst if src is None else src, dst_ref=dst,
                send_sem=send_sems.at[N_REL * a + k], recv_sem=recv_sems.at[N_REL * a + k],
                device_id=to, device_id_type=MESH)

        mine = [pltpu.make_async_copy(ins[a], outs[a].at[_slot(me)], local_sems.at[a]) for a in range(n)]
        for cp in mine:
            cp.start()
        first = []
        for a in range(n):
            first.append(copy(a, 0, me, sibling, src=ins[a]))
            first += [copy(a, 1 + j, me, (*chip, c), src=ins[a]) for j, chip in enumerate(chips)]
        for cp in first:
            cp.start()
        passed = []
        for j, chip in enumerate(chips):
            for a in range(n):
                copy(a, 1 + j, (*chip, c), me).wait_recv()
                fwd = copy(a, 4 + j, (*chip, c), sibling)
                fwd.start()
                passed.append(fwd)
        for a in range(n):
            copy(a, 0, sibling, me).wait_recv()
        for j, chip in enumerate(chips):
            for a in range(n):
                copy(a, 4 + j, (*chip, 1 - c), me).wait_recv()
        for cp in first + passed:
            cp.wait_send()
        for cp in mine:
            cp.wait()

    return pl.pallas_call(
        body, name=name,
        in_specs=[ANY] * n, out_specs=[ANY] * n,
        out_shape=[jax.ShapeDtypeStruct((N_DEV,) + a.shape, a.dtype) for a in arrays],
        scratch_shapes=[pltpu.SemaphoreType.DMA((N_REL * n,)), pltpu.SemaphoreType.DMA((N_REL * n,)),
                        pltpu.SemaphoreType.DMA((n,))],
    )(*arrays)


def _exchange(arrays, name):
    n = len(arrays)

    def body(*refs):
        ins, outs = refs[:n], refs[n:2 * n]
        send_sems, recv_sems, local_sems = refs[2 * n:]
        x, y, c = _place()
        me = _slot((x, y, c))
        mine = [pltpu.make_async_copy(ins[a].at[me], outs[a].at[me], local_sems.at[a]) for a in range(n)]
        for cp in mine:
            cp.start()
        copies = []
        for a in range(n):
            for k in range(1, N_DEV):
                flip = lambda v, bit: 1 - v if bit else v
                peer = (flip(x, k & 4), flip(y, k & 2), flip(c, k & 1))
                copies.append(pltpu.make_async_remote_copy(
                    src_ref=ins[a].at[_slot(peer)], dst_ref=outs[a].at[me],
                    send_sem=send_sems.at[N_REL * a + k - 1], recv_sem=recv_sems.at[N_REL * a + k - 1],
                    device_id=peer, device_id_type=MESH))
                copies[-1].start()
        i = 0
        for a in range(n):
            for k in range(1, N_DEV):
                flip = lambda v, bit: 1 - v if bit else v
                peer = (flip(x, k & 4), flip(y, k & 2), flip(c, k & 1))
                pltpu.make_async_remote_copy(
                    src_ref=ins[a].at[_slot(peer)], dst_ref=outs[a].at[_slot(peer)],
                    send_sem=send_sems.at[N_REL * a + k - 1], recv_sem=recv_sems.at[N_REL * a + k - 1],
                    device_id=peer, device_id_type=MESH).wait_recv()
                i += 1
        for cp in copies:
            cp.wait_send()
        for cp in mine:
            cp.wait()

    return pl.pallas_call(
        body, name=name,
        in_specs=[ANY] * n, out_specs=[ANY] * n,
        out_shape=[jax.ShapeDtypeStruct(a.shape, a.dtype) for a in arrays],
        scratch_shapes=[pltpu.SemaphoreType.DMA((N_REL * n,)), pltpu.SemaphoreType.DMA((N_REL * n,)),
                        pltpu.SemaphoreType.DMA((n,))],
    )(*arrays)


HBM = pl.BlockSpec(memory_space=pltpu.HBM)
SEM = pl.BlockSpec(memory_space=pltpu.SEMAPHORE)
EFFECT = pltpu.SideEffectType.DATAFLOW_SIDE_EFFECTING


def _peer_of(x, y, c, k):
    flip = lambda v, bit: 1 - v if bit else v
    return flip(x, k & 4), flip(y, k & 2), flip(c, k & 1)


def _view_whole(src, slot):
    return src


def _view_near(src, slot):
    return src


_view_near.peers = (1, 2, 4, 6)


def _view_block(src, slot):
    return src.at[slot]


W_IN_SHARD = W_IN_REF // N_DEV


def _view_window(rows):
    def view(src, slot):
        col0 = pl.multiple_of((W_IN_SHARD * slot // DH) * DH, DH)
        return src.at[pl.ds(rows[0], rows[1] - rows[0]), pl.ds(col0, D)]
    return view


def _split_copies(view, srcs, lands, send_sems, recv_sems, local_sems):
    x, y, c = _place()
    me = _slot((x, y, c))
    local, sends, waits = [], [], []
    for a, (src, land) in enumerate(zip(srcs, lands)):
        local.append(pltpu.make_async_copy(view(src, me), land.at[me], local_sems.at[a]))
        for k in getattr(view, "peers", range(1, N_DEV)):
            peer = _peer_of(x, y, c, k)
            mine = view(src, _slot(peer))
            sems = dict(send_sem=send_sems.at[N_REL * a + k - 1], recv_sem=recv_sems.at[N_REL * a + k - 1],
                        device_id=peer, device_id_type=MESH)
            sends.append(pltpu.make_async_remote_copy(src_ref=mine, dst_ref=land.at[me], **sems))
            waits.append(pltpu.make_async_remote_copy(src_ref=mine, dst_ref=land.at[_slot(peer)], **sems))
    return local, sends, waits


def _split_start(view, land_shapes, srcs, name, after):
    n = len(srcs)
    lands = [lax.empty(shp, s.dtype) for shp, s in zip(land_shapes, srcs)]

    def body(*refs):
        src_refs, land_refs = refs[:n], refs[n:2 * n]
        send_sems, recv_sems, local_sems = refs[2 * n + 1:2 * n + 4]
        token = refs[-1]
        local, sends, _ = _split_copies(view, src_refs, land_refs, send_sems, recv_sems, local_sems)
        for cp in local + sends:
            cp.start()
        token[...] = jnp.zeros_like(token)

    hbm = lambda a: pltpu.with_memory_space_constraint(a, pltpu.HBM)
    out = pl.pallas_call(
        body, name=name,
        out_shape=(pltpu.SemaphoreType.DMA((N_REL * n,)), pltpu.SemaphoreType.DMA((N_REL * n,)),
                   pltpu.SemaphoreType.DMA((n,)),
                   *[pltpu.HBM(s.shape, s.dtype) for s in srcs], *[pltpu.HBM(l.shape, l.dtype) for l in lands],
                   jax.ShapeDtypeStruct((8, DH), F32)),
        in_specs=[HBM] * (2 * n) + [ANY],
        out_specs=(SEM, SEM, SEM, *([HBM] * (2 * n)), pl.BlockSpec(memory_space=pltpu.VMEM)),
        input_output_aliases={i: 3 + i for i in range(2 * n)},
        compiler_params=pltpu.CompilerParams(has_side_effects=EFFECT),
    )(*[hbm(s) for s in srcs], *[hbm(l) for l in lands], after)
    handle = dict(view=view, n=n, sems=out[:3], srcs=list(out[3:3 + n]), lands=list(out[3 + n:3 + 2 * n]))
    return handle, out[-1]


def _split_wait(handle, name, after, srcs=None):
    view, n, sems, lands = handle["view"], handle["n"], handle["sems"], handle["lands"]
    srcs = handle["srcs"] if srcs is None else srcs
    afters = list(after) if isinstance(after, (list, tuple)) else [after]

    def body(*refs):
        src_refs, land_refs = refs[:n], refs[n:2 * n]
        send_sems, recv_sems, local_sems = refs[2 * n:2 * n + 3]
        local, _, waits = _split_copies(view, src_refs, land_refs, send_sems, recv_sems, local_sems)
        for cp in waits:
            cp.wait_send()
            cp.wait_recv()
        for cp in local:
            cp.wait()

    out = pl.pallas_call(
        body, name=name,
        out_shape=(*[pltpu.HBM(s.shape, s.dtype) for s in srcs], *[pltpu.HBM(l.shape, l.dtype) for l in lands]),
        in_specs=[HBM] * (2 * n) + [SEM, SEM, SEM] + [ANY] * len(afters),
        out_specs=tuple([HBM] * (2 * n)),
        input_output_aliases={i: i for i in range(2 * n)},
        compiler_params=pltpu.CompilerParams(has_side_effects=EFFECT),
    )(*srcs, *lands, *sems, *afters)
    handle["srcs"] = list(out[:n])
    return list(out[n:])


def _tie(x, token, name):
    def body(x_ref, t_ref, o_ref):
        pass

    return pl.pallas_call(
        body, name=name, out_shape=jax.ShapeDtypeStruct(x.shape, x.dtype),
        in_specs=[ANY, ANY], out_specs=ANY, input_output_aliases={0: 0},
    )(x, token)


def _forward_to_sibling(land, name):
    def body(land_ref, out_ref, send_sems, recv_sems):
        x, y, c = _place()
        sibling = (x, y, 1 - c)
        chips = [(1 - x, y), (x, 1 - y), (1 - x, 1 - y)]

        def copy(j, core):
            blk = _slot((*chips[j], core))
            return pltpu.make_async_remote_copy(src_ref=land_ref.at[blk], dst_ref=out_ref.at[blk],
                                                send_sem=send_sems.at[j], recv_sem=recv_sems.at[j],
                                                device_id=sibling, device_id_type=MESH)

        sends = [copy(j, c) for j in range(3)]
        for cp in sends:
            cp.start()
        for j in range(3):
            copy(j, 1 - c).wait_recv()
        for cp in sends:
            cp.wait_send()

    return pl.pallas_call(
        body, name=name, in_specs=[ANY], out_specs=ANY, input_output_aliases={0: 0},
        out_shape=jax.ShapeDtypeStruct(land.shape, land.dtype),
        scratch_shapes=[pltpu.SemaphoreType.DMA((3,)), pltpu.SemaphoreType.DMA((3,))],
    )(land)


def _mod_fwd(a, w, b):
    def body(a_ref, w_ref, b_ref, o_ref):
        o_ref[...] = _dot(_silu(a_ref[...]), w_ref[...], NN, precision=HI) + b_ref[...]

    return pl.pallas_call(
        body, name="mod_fwd", out_shape=jax.ShapeDtypeStruct((a.shape[0], w.shape[1]), F32),
        compiler_params=pltpu.CompilerParams(vmem_limit_bytes=VMEM_LIMIT),
    )(a, w, b)


def _mod_bwd(a, d, w):
    def body(a_ref, d_ref, w_ref, dw_ref, dc_ref):
        av = a_ref[...]
        dv = d_ref[...]
        dw_ref[...] = _dot(_silu(av), dv, TN, precision=HI)
        da = _dot(dv[0:8, :], w_ref[...], NT, precision=HI) * _dsilu(av[0:8, :])
        row = lax.broadcasted_iota(jnp.int32, da.shape, 0)
        dc_ref[...] = jnp.where(row == 0, da, 0.0)

    return pl.pallas_call(
        body, name="mod_bwd",
        out_shape=[jax.ShapeDtypeStruct(w.shape, F32), jax.ShapeDtypeStruct((8, w.shape[0]), F32)],
        compiler_params=pltpu.CompilerParams(vmem_limit_bytes=VMEM_LIMIT),
    )(a, d, w)


def _sum_devices(g):
    def body(g_ref, o_ref):
        acc = g_ref[0]
        for i in range(1, g.shape[0]):
            acc = acc + g_ref[i]
        o_ref[...] = acc

    return pl.pallas_call(body, name="sum_devices_%d" % g.shape[1],
                          out_shape=jax.ShapeDtypeStruct(g.shape[1:], F32))(g)


def _sum_windows(g, name):
    n, r, c = g.shape
    tr = 128

    def body(g_ref, o_ref):
        x, y, cc = _place()
        lane0 = (W_IN_SHARD * _slot((x, y, cc))) % DH
        acc = g_ref[0].astype(F32)
        for i in range(1, n):
            acc = acc + g_ref[i].astype(F32)
        o_ref[...] = pltpu.roll(acc, (c - lane0) % c, 1).T

    return pl.pallas_call(
        body, name=name, grid=(r // tr,),
        in_specs=[pl.BlockSpec((n, tr, c), lambda i: (0, i, 0))],
        out_specs=pl.BlockSpec((c, tr), lambda i: (0, i)),
        out_shape=jax.ShapeDtypeStruct((c, r), F32),
        compiler_params=_cp("parallel"),
    )(g)


def _adam_rows(r, c, n):
    budget = 6 * 1024 * 1024
    best = None
    for tr in range(16, r + 1, 16):
        if r % tr == 0 and tr * c * (2 * n + 28) <= budget:
            best = tr
    return best if best is not None else r


def _adamw(g, w, m, v, name):
    n, r, c = g.shape
    tr = _adam_rows(r, c, n)
    bc1 = 1.0 - ADAM_B1 ** ADAM_STEP
    bc2 = 1.0 - ADAM_B2 ** ADAM_STEP

    def body(g_ref, w_ref, m_ref, v_ref, go_ref, d_ref, mo_ref, vo_ref):
        grad = g_ref[0].astype(F32)
        for i in range(1, n):
            grad = grad + g_ref[i].astype(F32)
        go_ref[...] = grad
        m_new = ADAM_B1 * m_ref[...] + (1.0 - ADAM_B1) * grad
        v_new = ADAM_B2 * v_ref[...] + (1.0 - ADAM_B2) * (grad * grad)
        mo_ref[...] = m_new
        vo_ref[...] = v_new
        d_ref[...] = -ADAM_LR * ((m_new / bc1) / (jnp.sqrt(v_new / bc2) + ADAM_EPS) + ADAM_WD * w_ref[...])

    blk = pl.BlockSpec((tr, c), lambda i: (i, 0))
    out = jax.ShapeDtypeStruct((r, c), F32)
    return pl.pallas_call(
        body, name=name, grid=(r // tr,),
        in_specs=[pl.BlockSpec((n, tr, c), lambda i: (0, i, 0)), blk, blk, blk],
        out_specs=[blk] * 4, out_shape=[out] * 4,
        compiler_params=_cp("parallel"),
    )(g, w, m, v)


def kernel(x, c, ctx, c_ctx, w_mod, b_mod, norm_pre1, norm_post1, norm_pre2, norm_post2, w_in, hg_lb, hg_onorm, gla_w_gk, gla_b_gk, gla_onorm, w_br_hg, w_br_gla, w_out, w_ff_gate, w_ff_up, w_ff_down, loss_target, m_c_ctx, m_w_mod, m_b_mod, m_norm_pre1, m_norm_post1, m_norm_pre2, m_norm_post2, m_w_in, m_hg_lb, m_hg_onorm, m_gla_w_gk, m_gla_b_gk, m_gla_onorm, m_w_br_hg, m_w_br_gla, m_w_out, m_w_ff_gate, m_w_ff_up, m_w_ff_down, v_c_ctx, v_w_mod, v_b_mod, v_norm_pre1, v_norm_post1, v_norm_pre2, v_norm_post2, v_w_in, v_hg_lb, v_hg_onorm, v_gla_w_gk, v_gla_b_gk, v_gla_onorm, v_w_br_hg, v_w_br_gla, v_w_out, v_w_ff_gate, v_w_ff_up, v_w_ff_down):
    xi, yi, ci = lax.axis_index("x"), lax.axis_index("y"), lax.axis_index("c")
    me = 4 * xi + 2 * yi + ci
    t = CTX + x.shape[1]

    c_all, lb_g, wgk_g, bgk_g = _all_gather([c, hg_lb, gla_w_gk[0], gla_b_gk[0]], "ag_small")
    tr_ = lambda a: jnp.swapaxes(a[0], 0, 1)
    big = [w_in[0], w_br_hg[0], w_br_gla[0], w_out[0], tr_(w_ff_gate), tr_(w_ff_up), w_ff_down[0]]
    big_bf = [w.astype(BF16) for w in big]
    cols = lambda g: jnp.transpose(g, (1, 0, 2)).reshape(g.shape[1], N_DEV * g.shape[2])

    def get_w_in(after):
        land, = _split_wait(w_in_handle, "ag_w_in_wait", after)
        return _layout_w_in(cols(_forward_to_sibling(land, "ag_w_in_forward")))

    def get_mix(after):
        g_brh, g_brg, g_out = _split_wait(mix_handle, "ag_mix_wait", after)
        return _gate_cols(cols(g_brh)), _gate_cols(cols(g_brg)), _gate_rows(g_out.reshape(D, D))

    def get_ffn(after):
        g_gate, g_up, g_down = _split_wait(ffn_handle, "ag_ffn_wait", after)
        return (g_gate.reshape(D_FF, D), g_up.reshape(D_FF, D)), g_down.reshape(D_FF, D)

    hg_lb_full = jnp.transpose(lb_g, (1, 2, 0, 3)).reshape(2, 2, HW)
    wgk_k = _layout_wgk(jnp.transpose(wgk_g, (1, 2, 0, 3)).reshape(2, 16, HW)).astype(BF16)
    bgk_k = jnp.transpose(bgk_g, (1, 0, 2)).reshape(1, D)
    onw = jnp.concatenate([jnp.tile(hg_onorm, (1, NH // 2)), jnp.tile(gla_onorm, (1, NH // 2))], axis=1)

    n_mod = w_mod.shape[2]
    a9 = jnp.concatenate([c_ctx[None], c_all[:, 0], jnp.zeros((16 - 1 - N_DEV, D), F32)], axis=0)
    b_loc = lax.dynamic_slice(b_mod, (0, me * n_mod), (1, n_mod))
    s_loc = _mod_fwd(a9, w_mod[0], b_loc)
    s_all, = _all_gather([s_loc], "ag_mod")
    mod_all = jnp.transpose(s_all, (1, 0, 2)).reshape(16, N_DEV * n_mod)
    pad8 = lambda m: jnp.concatenate([m.reshape(6, D), jnp.zeros((2, D), F32)], axis=0)
    modc = pad8(mod_all[0])
    modx = pad8(lax.dynamic_slice(mod_all, (1 + me, 0), (1, N_DEV * n_mod))[0])

    gathered = lambda arrs: [(N_DEV,) + a.shape for a in arrs]
    w_in_handle, tok = _split_start(_view_near, gathered(big_bf[:1]), big_bf[:1], "ag_w_in_start", s_all)
    mix_handle, tok = _split_start(_view_whole, gathered(big_bf[1:4]), big_bf[1:4], "ag_mix_start", tok)
    ffn_handle, tok = _split_start(_view_whole, gathered(big_bf[4:]), big_bf[4:], "ag_ffn_start", tok)

    z = _tie(jnp.concatenate([ctx[0], x[0]], axis=0), tok, "tie_z")
    norms = (norm_pre1, norm_post1, norm_pre2, norm_post2)
    shard = lambda d: jnp.transpose(d.reshape(d.shape[0], N_DEV, -1), (1, 0, 2)).astype(BF16)
    rowshard = lambda d: d.reshape(N_DEV, d.shape[0] // N_DEV, d.shape[1]).astype(BF16)
    sent, w_in_grad = [], {}

    def send_w_in(i, x_after):
        half, rows = W_IN_GRAD_CHUNKS[i]
        handle, tok = _split_start(_view_window(rows), [(N_DEV, rows[1] - rows[0], D)], w_in_grad[half],
                                   "grads_w_in%d_start" % i, x_after)
        w_in_grad[half] = handle["srcs"]
        sent.append(("w_in%d" % i, ["w_in#%d" % i], handle))
        return _tie(x_after, tok, "tie_w_in%d" % i)

    def send(names, grads, x_after):
        if names == ("w_in_a",):
            w_in_grad["a"] = list(grads)
            return send_w_in(0, x_after)
        if names == ("w_in_b",):
            w_in_grad["b"] = list(grads)
            return x_after
        arrs, leaves = [], []
        for nm, g in zip(names, grads):
            if nm in ("w_gate_t", "w_up_t"):
                arrs.append(rowshard(g))
                leaves.append({"w_gate_t": "w_ff_gate", "w_up_t": "w_ff_up"}[nm])
            elif nm == "w_down":
                arrs.append(rowshard(g))
                leaves.append("w_ff_down")
            elif nm == "w_out":
                arrs.append(rowshard(g[GOFF:GOFF + D]))
                leaves.append(nm)
            else:
                arrs.append(shard(g[:, GOFF:GOFF + D]))
                leaves.append(nm)
        handle, tok = _split_start(_view_block, [a.shape for a in arrs], arrs, "grads_%s_start" % names[0], x_after)
        sent.append((names[0], leaves, handle))
        return _tie(x_after, tok, "tie_" + names[0])

    r = _local_step(z, loss_target[0], modc, modx, norms, onw, hg_lb_full, wgk_k, bgk_k,
                    get_w_in, get_mix, get_ffn, send)
    grad_x = r["grad_x"][None]

    sm_pre, sm_mid, sm_fin = r["sm_pre"], r["sm_mid"], r["sm_final"]
    dmodc = jnp.stack([sm_pre[0], sm_pre[2], sm_mid[4], sm_mid[0], sm_mid[2], sm_fin[0]]).reshape(-1)
    dmodx = jnp.stack([sm_pre[1], sm_pre[3], sm_mid[5], sm_mid[1], sm_mid[3], sm_fin[1]]).reshape(-1)
    on = r["sm_post"][0].reshape(NH, DH)
    pieces = [dmodc, dmodx, sm_pre[4], sm_mid[7], sm_mid[6], sm_fin[2], on[:NH // 2].sum(0), on[NH // 2:].sum(0),
              r["d_lb"][:2].reshape(-1), _unlayout_wgk(r["d_wgk"]).reshape(-1), r["d_bgk"][0]]
    loss_local = (0.5 / D) * jnp.sum(r["loss_vec"])
    pieces.append(jnp.concatenate([loss_local.reshape(1), jnp.zeros((DH - 1,), F32)]))
    sizes = [p.shape[0] for p in pieces]
    pack = jnp.concatenate(pieces).reshape(-1, DH)
    pack_all, = _all_gather([pack], "ag_small_grads")
    pack_all = send_w_in(1, pack_all)
    tot = _sum_devices(pack_all).reshape(-1)
    offs = [sum(sizes[:i]) for i in range(len(sizes))]
    part = lambda i: tot[offs[i]:offs[i] + sizes[i]]
    dmodc_t, dmodx_t = part(0), part(1)
    g_b_mod = (dmodc_t + dmodx_t)[None]
    g_norms = [part(i)[None] for i in (2, 3, 4, 5)]
    g_hg_on, g_gla_on = part(6)[None], part(7)[None]
    lb0 = lax.dynamic_slice(part(8).reshape(2, HW), (0, me * (HW // N_DEV)), (2, HW // N_DEV))
    g_hg_lb = jnp.stack([lb0, -lb0])
    g_wgk = lax.dynamic_slice(part(9).reshape(2, 16, HW), (0, 0, me * (HW // N_DEV)), (2, 16, HW // N_DEV))[None]
    g_bgk = lax.dynamic_slice(part(10).reshape(2, HW), (0, me * (HW // N_DEV)), (2, HW // N_DEV))[None]
    loss = part(11)[0]

    dmx_all = pack_all.reshape(N_DEV, -1)[:, sizes[0]:sizes[0] + sizes[1]]
    d9 = jnp.concatenate([lax.dynamic_slice(dmodc_t[None], (0, me * n_mod), (1, n_mod)),
                          lax.dynamic_slice(dmx_all, (0, me * n_mod), (N_DEV, n_mod)),
                          jnp.zeros((16 - 1 - N_DEV, n_mod), F32)], axis=0)
    g_w_mod, dcc_part = _mod_bwd(a9, d9, w_mod[0])
    dcc_all, = _all_gather([dcc_part], "ag_c_ctx")
    dcc_all = send_w_in(2, dcc_all)
    g_c_ctx = _sum_devices(dcc_all)[0]

    recv = {}
    for first, leaves, handle in sent:
        if not first.startswith("w_in"):
            recv.update(zip(leaves, _split_wait(handle, "grads_%s_wait" % first, g_c_ctx)))
    moms = [(m_w_in, v_w_in), (m_w_br_hg, v_w_br_hg), (m_w_br_gla, v_w_br_gla), (m_w_out, v_w_out),
            (m_w_ff_gate, v_w_ff_gate), (m_w_ff_up, v_w_ff_up), (m_w_ff_down, v_w_ff_down)]
    names = ["w_in", "w_br_hg", "w_br_gla", "w_out", "w_ff_gate", "w_ff_up", "w_ff_down"]
    res = {}

    def update(nm, w, m, v):
        if nm in ("w_ff_gate", "w_ff_up"):
            outs = _adamw(recv[nm], w, tr_(m), tr_(v), "adamw_" + nm)
            res[nm] = [jnp.swapaxes(o, 0, 1)[None] for o in outs]
        else:
            res[nm] = [o[None] for o in _adamw(recv[nm], w, m[0], v[0], "adamw_" + nm)]

    for nm, w, (m, v) in list(zip(names, big, moms))[1:]:
        update(nm, w, m, v)
    res["w_mod"] = [o[None] for o in _adamw(g_w_mod[None], w_mod[0], m_w_mod[0], v_w_mod[0], "adamw_w_mod")]

    small = [("c_ctx", c_ctx, m_c_ctx, v_c_ctx, g_c_ctx), ("b_mod", b_mod, m_b_mod, v_b_mod, g_b_mod),
             ("norm_pre1", norm_pre1, m_norm_pre1, v_norm_pre1, g_norms[0]),
             ("norm_post1", norm_post1, m_norm_post1, v_norm_post1, g_norms[1]),
             ("norm_pre2", norm_pre2, m_norm_pre2, v_norm_pre2, g_norms[2]),
             ("norm_post2", norm_post2, m_norm_post2, v_norm_post2, g_norms[3]),
             ("hg_lb", hg_lb, m_hg_lb, v_hg_lb, g_hg_lb), ("hg_onorm", hg_onorm, m_hg_onorm, v_hg_onorm, g_hg_on),
             ("gla_w_gk", gla_w_gk, m_gla_w_gk, v_gla_w_gk, g_wgk), ("gla_b_gk", gla_b_gk, m_gla_b_gk, v_gla_b_gk, g_bgk),
             ("gla_onorm", gla_onorm, m_gla_onorm, v_gla_onorm, g_gla_on)]
    flat = lambda k: jnp.concatenate([s[k].reshape(-1) for s in small]).reshape(-1, DH)
    outs = _adamw(flat(4)[None], flat(1), flat(2), flat(3), "adamw_small")
    off = 0
    for nm, w, _, _, _ in small:
        res[nm] = [o.reshape(-1)[off:off + w.size].reshape(w.shape) for o in outs]
        off += w.size

    done = [res[nm][0] for nm in names[1:]] + [res["w_mod"][0], outs[0]]
    sums = []
    for i, (first, leaves, handle) in enumerate(s for s in sent if s[0].startswith("w_in")):
        half = W_IN_GRAD_CHUNKS[i][0]
        land, = _split_wait(handle, "grads_%s_wait" % first, done, srcs=w_in_grad[half])
        w_in_grad[half] = handle["srcs"]
        sums.append(_sum_windows(land, "sum_windows%d" % i))
    g_t = jnp.concatenate(sums, axis=1)[:W_IN_SHARD]
    lin = lambda a: a.reshape(W_IN_SHARD * D // DH, DH)
    major = lambda a: lin(jnp.transpose(a, (2, 0, 1)))
    outs = _adamw(lin(g_t)[None], major(w_in), major(m_w_in), major(v_w_in), "adamw_w_in")
    res["w_in"] = [jnp.transpose(o.reshape(W_IN_SHARD, 1, D), (1, 2, 0)) for o in outs]

    order = ["c_ctx", "w_mod", "b_mod", "norm_pre1", "norm_post1", "norm_pre2", "norm_post2", "w_in", "hg_lb",
             "hg_onorm", "gla_w_gk", "gla_b_gk", "gla_onorm", "w_br_hg", "w_br_gla", "w_out", "w_ff_gate", "w_ff_up",
             "w_ff_down"]
    return (loss, grad_x, *[res[n][k] for k in range(4) for n in order])
```

```python
import functools

import jax
import jax.numpy as jnp
from jax import lax
from jax.experimental import pallas as pl
from jax.experimental.pallas import tpu as pltpu

F32 = jnp.float32
BF16 = jnp.bfloat16
HI = lax.Precision.HIGHEST

N_DEV = 8
D = 1024
CTX = 256
HW = 512
DH = 128
NH = 8
D_FF = 2816
EPS = 1e-6
GLA_NORM = 16.0
CHUNK = 64
TR = 256
NCT = CTX // TR
W_IN_COLS = 7168
MAIN0 = 0
LR0 = 4608
GW = 1152
GOFF = 32
GATE_HG0 = LR0
GATE_GLA0 = LR0 + D
LEVELS = (32, 16, 8)
EXP_CLAMP = 80.0
VMEM_LIMIT = 48 * 1024 * 1024

ADAM_LR, ADAM_B1, ADAM_B2, ADAM_EPS, ADAM_WD, ADAM_STEP = 0.001, 0.9, 0.999, 1e-08, 0.01, 10


def _cp(*sem):
    return pltpu.CompilerParams(dimension_semantics=sem, vmem_limit_bytes=VMEM_LIMIT)


def _sig(x):
    return jax.nn.sigmoid(x)


def _silu(x):
    return x * _sig(x)


def _dsilu(x):
    s = _sig(x)
    return s * (1.0 + x * (1.0 - s))


def _rstd(x):
    return lax.rsqrt(jnp.mean(x * x, axis=-1, keepdims=True) + EPS)


def _rms_bwd(a, y, r):
    return r * (a - y * (r * r) * jnp.mean(a * y, axis=-1, keepdims=True))


def _colsum(x):
    return jnp.sum(x, axis=0, keepdims=True)


def _dot(a, b, dims, precision=None):
    return lax.dot_general(a, b, (dims, ((), ())), preferred_element_type=F32, precision=precision)


NN = ((1,), (0,))
NT = ((1,), (1,))
TN = ((0,), (0,))

SCAN_HEADS_FWD = 4
SCAN_HEADS_BWD = 4


def _split_dot(m, x):
    mb = m.astype(BF16)
    x1 = x.astype(BF16)
    r1 = x - x1.astype(F32)
    x2 = r1.astype(BF16)
    x3 = (r1 - x2.astype(F32)).astype(BF16)
    return _dot(mb, x1, NN) + _dot(mb, x2, NN) + _dot(mb, x3, NN)


def _matmul(a, b, dims, out_dtype, name, tm, tn, tk, a_off=0, m_out=None):
    pair = isinstance(b, (tuple, list))
    bs = list(b) if pair else [b]
    b1 = bs[0]
    rows = b1.shape[0] * len(bs)
    half = None
    if dims == NN:
        m, k, n = a.shape[0], rows, b1.shape[1]
        a_spec = pl.BlockSpec((tm, tk), lambda i, j, kk: (i, kk + a_off))
        half = b1.shape[0] // tk
        b_maps = [lambda i, j, kk: (kk, j)] if not pair else [
            lambda i, j, kk: (jnp.minimum(kk, half - 1), j), lambda i, j, kk: (jnp.maximum(kk - half, 0), j)]
        b_specs = [pl.BlockSpec((tk, tn), f) for f in b_maps]
        axis = 2
    elif dims == NT:
        m, k, n = a.shape[0], b1.shape[1], rows
        a_spec = pl.BlockSpec((tm, tk), lambda i, j, kk: (i, kk + a_off))
        half = b1.shape[0] // tn
        b_maps = [lambda i, j, kk: (j, kk)] if not pair else [
            lambda i, j, kk: (jnp.minimum(j, half - 1), kk), lambda i, j, kk: (jnp.maximum(j - half, 0), kk)]
        b_specs = [pl.BlockSpec((tn, tk), f) for f in b_maps]
        axis = 1
    else:
        assert not pair
        m, k = (a.shape[1] if m_out is None else m_out), a.shape[0]
        n = b1.shape[1]
        a_spec = pl.BlockSpec((tk, tm), lambda i, j, kk: (kk, i + a_off))
        b_specs = [pl.BlockSpec((tk, tn), lambda i, j, kk: (kk, j))]
    assert m % tm == 0 and n % tn == 0 and k % tk == 0, (name, m, n, k, tm, tn, tk)
    nk = k // tk
    nb = len(bs)

    def body(a_ref, *refs):
        o_ref = refs[nb]
        if pair:
            bv = jnp.where(pl.program_id(axis) < half, refs[0][...], refs[1][...])
        else:
            bv = refs[0][...]
        part = _dot(a_ref[...], bv, dims)
        if nk == 1:
            o_ref[...] = part.astype(o_ref.dtype)
            return
        acc_ref = refs[nb + 1]
        kk = pl.program_id(2)

        @pl.when(kk == 0)
        def _():
            acc_ref[...] = part

        @pl.when(kk > 0)
        def _():
            acc_ref[...] += part

        @pl.when(kk == nk - 1)
        def _():
            o_ref[...] = acc_ref[...].astype(o_ref.dtype)

    return pl.pallas_call(
        body,
        name=name,
        grid=(m // tm, n // tn, nk),
        in_specs=[a_spec] + b_specs,
        out_specs=pl.BlockSpec((tm, tn), lambda i, j, kk: (i, j)),
        out_shape=jax.ShapeDtypeStruct((m, n), out_dtype),
        scratch_shapes=[] if nk == 1 else [pltpu.VMEM((tm, tn), F32)],
        compiler_params=_cp("parallel", "parallel", "arbitrary"),
    )(a, *bs)


def _row(c):
    return pl.BlockSpec((TR, c), lambda i: (i, 0))


def _rowcol(width, cb):
    return pl.BlockSpec((TR, width), lambda i: (i, cb))


def _full(shape):
    return pl.BlockSpec(shape, lambda i: (0,) * len(shape))


def _mod_row(mc_ref, mx_ref, k, is_ctx):
    return jnp.where(is_ctx, mc_ref[k:k + 1, :], mx_ref[k:k + 1, :])


def _z_specs():
    return [pl.BlockSpec((TR, D), lambda i: (jnp.minimum(i, NCT - 1), 0)),
            pl.BlockSpec((TR, D), lambda i: (jnp.maximum(i - NCT, 0), 0))]


def _z_tile(c_ref, x_ref, is_ctx):
    return jnp.where(is_ctx, c_ref[...], x_ref[...])


def _acc_row(ref, k, val):
    ref[k:k + 1, :] += val


def _acc_mod(ref, k, is_ctx, val):
    zero = jnp.zeros_like(val)
    ref[k:k + 1, :] += jnp.where(is_ctx, val, zero)
    ref[k + 1:k + 2, :] += jnp.where(is_ctx, zero, val)


def _prenorm(z, nw, modc, modx, i_shift, i_scale, name):
    t = z[0].shape[0] + z[1].shape[0]

    def body(zc_ref, zx_ref, nw_ref, mc_ref, mx_ref, h_ref):
        is_ctx = pl.program_id(0) < NCT
        x = _z_tile(zc_ref, zx_ref, is_ctx)
        n = x * _rstd(x) * nw_ref[...]
        h = n * (1.0 + _mod_row(mc_ref, mx_ref, i_scale, is_ctx)) + _mod_row(mc_ref, mx_ref, i_shift, is_ctx)
        h_ref[...] = h.astype(BF16)

    return pl.pallas_call(
        body, name=name, grid=(t // TR,),
        in_specs=_z_specs() + [_full((1, D)), _full((8, D)), _full((8, D))],
        out_specs=_row(D),
        out_shape=jax.ShapeDtypeStruct((t, D), BF16),
        compiler_params=_cp("parallel"),
    )(*z, nw, modc, modx)


def _hg_lb(lb_ref, d):
    a0 = lb_ref[0, d:d + 1, :]
    a1 = lb_ref[1, d:d + 1, :]
    mx = jnp.maximum(a0, a1)
    e0 = jnp.exp(a0 - mx)
    e1 = jnp.exp(a1 - mx)
    return e0 / (e0 + e1)


def _log_sigmoid(x):
    return jnp.minimum(x, 0.0) - jnp.log(1.0 + jnp.exp(-jnp.abs(x)))


def _gates_fwd(p, hg_lb, wgk, bgk):
    t = p.shape[0]
    seg = lambda j: _rowcol(HW, MAIN0 // HW + j)

    def body(hq_ref, hi_ref, hf_ref, hb_ref, gq_ref, gk_ref, gv_ref, lr_ref, lb_ref, wgk_ref, bgk_ref,
             q_ref, v_ref, kf_ref, kb_ref, gf_ref, gb_ref):
        q_ref[:, :HW] = _silu(hq_ref[...].astype(F32)).astype(BF16)
        q_ref[:, HW:] = (gq_ref[...].astype(F32) * (DH ** -0.5)).astype(BF16)
        v_ref[:, :HW] = hi_ref[...]
        v_ref[:, HW:] = gv_ref[...]
        xg = _dot(lr_ref[...].astype(BF16), wgk_ref[...], NN) + bgk_ref[...]
        for d, (raw_ref, k_ref, g_ref) in enumerate(((hf_ref, kf_ref, gf_ref), (hb_ref, kb_ref, gb_ref))):
            lbd = _hg_lb(lb_ref, d)
            f = lbd + (1.0 - lbd) * _sig(raw_ref[...].astype(F32))
            k_ref[:, :HW] = (1.0 - f).astype(BF16)
            k_ref[:, HW:] = gk_ref[...]
            g_ref[:, :HW] = jnp.log(f)
            g_ref[:, HW:] = _log_sigmoid(xg[:, d * HW:(d + 1) * HW]) * (1.0 / GLA_NORM)

    out = jax.ShapeDtypeStruct((t, D), F32)
    outb = jax.ShapeDtypeStruct((t, D), BF16)
    return pl.pallas_call(
        body, name="gates_fwd", grid=(t // TR,),
        in_specs=[seg(0), seg(1), seg(2), seg(3), seg(5), seg(6), seg(7), _rowcol(DH, LR0 // DH),
                  _full((2, 2, HW)), _full((DH, D)), _full((1, D))],
        out_specs=[_row(D)] * 6,
        out_shape=[outb] * 4 + [out] * 2,
        compiler_params=_cp("parallel"),
    )(p, p, p, p, p, p, p, p, hg_lb, wgk, bgk)


def _post_fwd(o_fw, o_bw, p, onw):
    t = o_fw.shape[0]

    def body(of_ref, ob_ref, g1_ref, g2_ref, w_ref, y_ref):
        for h in range(NH):
            sl = slice(h * DH, (h + 1) * DH)
            o = of_ref[:, sl] + ob_ref[:, sl]
            g_ref = g1_ref if h < NH // 2 else g2_ref
            gs = slice((h % (NH // 2)) * DH, (h % (NH // 2) + 1) * DH)
            n = o * _rstd(o) * w_ref[:, sl]
            y_ref[:, sl] = (n * _silu(g_ref[:, gs].astype(F32))).astype(BF16)

    return pl.pallas_call(
        body, name="post_fwd", grid=(t // TR,),
        in_specs=[_row(D), _row(D), _rowcol(HW, MAIN0 // HW + 4), _rowcol(HW, MAIN0 // HW + 8), _full((1, D))],
        out_specs=_row(D),
        out_shape=jax.ShapeDtypeStruct((t, D), BF16),
        compiler_params=_cp("parallel"),
    )(o_fw, o_bw, p, p, onw)


def _gate_window_specs(col0):
    return [_rowcol(HW, col0 // HW), _rowcol(HW, col0 // HW + 1), _rowcol(DH, (col0 + 2 * HW) // DH)]


def _gate_window(refs):
    return jnp.concatenate([r[...].astype(F32) for r in refs], axis=1)


def _merge_fwd(p, u1, u2):
    t = p.shape[0]

    def body(a0, a1, a2, b0, b1, b2, u1_ref, u2_ref, m_ref):
        f = lambda r: r[...].astype(F32)
        m_ref[...] = (_sig(_gate_window((a0, a1, a2))) * f(u1_ref)
                      + _sig(_gate_window((b0, b1, b2))) * f(u2_ref)).astype(BF16)

    return pl.pallas_call(
        body, name="merge_fwd", grid=(t // TR,),
        in_specs=_gate_window_specs(GATE_HG0) + _gate_window_specs(GATE_GLA0) + [_row(GW), _row(GW)],
        out_specs=_row(GW),
        out_shape=jax.ShapeDtypeStruct((t, GW), BF16),
        compiler_params=_cp("parallel"),
    )(p, p, p, p, p, p, u1, u2)


def _mid_fwd(z, y1, nw_post, nw_pre, modc, modx):
    t = y1.shape[0]

    def body(zc_ref, zx_ref, y_ref, wpo_ref, wpr_ref, mc_ref, mx_ref, z1_ref, h_ref):
        is_ctx = pl.program_id(0) < NCT
        y = y_ref[...].astype(F32)
        z1 = _z_tile(zc_ref, zx_ref, is_ctx) + _mod_row(mc_ref, mx_ref, 2, is_ctx) * (y * _rstd(y) * wpo_ref[...])
        z1_ref[...] = z1
        n = z1 * _rstd(z1) * wpr_ref[...]
        h = n * (1.0 + _mod_row(mc_ref, mx_ref, 4, is_ctx)) + _mod_row(mc_ref, mx_ref, 3, is_ctx)
        h_ref[...] = h.astype(BF16)

    return pl.pallas_call(
        body, name="mid_fwd", grid=(t // TR,),
        in_specs=_z_specs() + [_row(D), _full((1, D)), _full((1, D)), _full((8, D)), _full((8, D))],
        out_specs=[_row(D), _row(D)],
        out_shape=[jax.ShapeDtypeStruct((t, D), F32), jax.ShapeDtypeStruct((t, D), BF16)],
        compiler_params=_cp("parallel"),
    )(*z, y1, nw_post, nw_pre, modc, modx)


def _swiglu_fwd(uv):
    t = uv.shape[0]

    def body(u_ref, v_ref, a_ref):
        a_ref[...] = (_silu(u_ref[...].astype(F32)) * v_ref[...].astype(F32)).astype(BF16)

    return pl.pallas_call(
        body, name="swiglu_fwd", grid=(t // TR,),
        in_specs=[_rowcol(D_FF, 0), _rowcol(D_FF, 1)],
        out_specs=_row(D_FF),
        out_shape=jax.ShapeDtypeStruct((t, D_FF), BF16),
        compiler_params=_cp("parallel"),
    )(uv, uv)


def _swiglu_bwd(uv, da):
    t = uv.shape[0]

    def body(u_ref, v_ref, da_ref, d_ref):
        u = u_ref[...].astype(F32)
        d = da_ref[...].astype(F32)
        d_ref[:, :D_FF] = (d * v_ref[...].astype(F32) * _dsilu(u)).astype(BF16)
        d_ref[:, D_FF:] = (d * _silu(u)).astype(BF16)

    return pl.pallas_call(
        body, name="swiglu_bwd", grid=(t // TR,),
        in_specs=[_rowcol(D_FF, 0), _rowcol(D_FF, 1), _row(D_FF)],
        out_specs=_row(2 * D_FF),
        out_shape=jax.ShapeDtypeStruct((t, 2 * D_FF), BF16),
        compiler_params=_cp("parallel"),
    )(uv, uv, da)


def _final(z1, y2, target, nw, modc, modx):
    t = z1.shape[0]

    def body(z1_ref, y_ref, tg_ref, w_ref, mc_ref, mx_ref, dz_ref, dy_ref, loss_ref, sm_ref):
        i = pl.program_id(0)
        is_ctx = i < NCT

        @pl.when(i == 0)
        def _():
            loss_ref[...] = jnp.zeros_like(loss_ref)
            sm_ref[...] = jnp.zeros_like(sm_ref)

        g = _mod_row(mc_ref, mx_ref, 5, is_ctx)
        y = y_ref[...].astype(F32)
        r = _rstd(y)
        w = w_ref[...]
        yr = y * r
        n = yr * w
        e = z1_ref[...] + g * n - tg_ref[...]
        lat = jnp.where(is_ctx, 0.0, 1.0)
        loss_ref[...] += lat * _colsum(e * e)
        dz = e * (lat / D)
        dz_ref[...] = dz
        _acc_mod(sm_ref, 0, is_ctx, _colsum(dz * n))
        dn = dz * g
        _acc_row(sm_ref, 2, _colsum(dn * yr))
        dy_ref[...] = _rms_bwd(dn * w, y, r).astype(BF16)

    return pl.pallas_call(
        body, name="final", grid=(t // TR,),
        in_specs=[_row(D), _row(D), pl.BlockSpec((TR, D), lambda i: (jnp.maximum(i - NCT, 0), 0)),
                  _full((1, D)), _full((8, D)), _full((8, D))],
        out_specs=[_row(D), _row(D), _full((1, D)), _full((8, D))],
        out_shape=[jax.ShapeDtypeStruct((t, D), F32), jax.ShapeDtypeStruct((t, D), BF16),
                   jax.ShapeDtypeStruct((1, D), F32), jax.ShapeDtypeStruct((8, D), F32)],
        compiler_params=_cp("arbitrary"),
    )(z1, y2, target, nw, modc, modx)


def _mid_bwd(dh2, dz, z1, y1, nw_post, nw_pre, modc, modx):
    t = z1.shape[0]

    def body(dh_ref, dz_ref, z1_ref, y_ref, wpo_ref, wpr_ref, mc_ref, mx_ref, dzo_ref, dy_ref, sm_ref):
        i = pl.program_id(0)
        is_ctx = i < NCT

        @pl.when(i == 0)
        def _():
            sm_ref[...] = jnp.zeros_like(sm_ref)

        dh = dh_ref[...].astype(F32)
        z1 = z1_ref[...]
        r = _rstd(z1)
        zr = z1 * r
        wpr = wpr_ref[...]
        n = zr * wpr
        _acc_mod(sm_ref, 0, is_ctx, _colsum(dh))
        _acc_mod(sm_ref, 2, is_ctx, _colsum(dh * n))
        dn = dh * (1.0 + _mod_row(mc_ref, mx_ref, 4, is_ctx))
        _acc_row(sm_ref, 6, _colsum(dn * zr))
        dz1 = dz_ref[...] + _rms_bwd(dn * wpr, z1, r)
        dzo_ref[...] = dz1
        y = y_ref[...].astype(F32)
        r1 = _rstd(y)
        yr = y * r1
        wpo = wpo_ref[...]
        g = _mod_row(mc_ref, mx_ref, 2, is_ctx)
        _acc_mod(sm_ref, 4, is_ctx, _colsum(dz1 * (yr * wpo)))
        dn1 = dz1 * g
        _acc_row(sm_ref, 7, _colsum(dn1 * yr))
        dy_ref[...] = _rms_bwd(dn1 * wpo, y, r1).astype(BF16)

    return pl.pallas_call(
        body, name="mid_bwd", grid=(t // TR,),
        in_specs=[_row(D)] * 4 + [_full((1, D)), _full((1, D)), _full((8, D)), _full((8, D))],
        out_specs=[_row(D), _row(D), _full((8, D))],
        out_shape=[jax.ShapeDtypeStruct((t, D), F32), jax.ShapeDtypeStruct((t, D), BF16),
                   jax.ShapeDtypeStruct((8, D), F32)],
        compiler_params=_cp("arbitrary"),
    )(dh2, dz, z1, y1, nw_post, nw_pre, modc, modx)


def _pre_bwd(dh1, dz, z, nw, modc, modx):
    t = dh1.shape[0]

    def body(dh_ref, dz_ref, zc_ref, zx_ref, w_ref, mc_ref, mx_ref, dzo_ref, sm_ref):
        i = pl.program_id(0)
        is_ctx = i < NCT

        @pl.when(i == 0)
        def _():
            sm_ref[...] = jnp.zeros_like(sm_ref)

        dh = dh_ref[...].astype(F32)
        x = _z_tile(zc_ref, zx_ref, is_ctx)
        r = _rstd(x)
        xr = x * r
        w = w_ref[...]
        _acc_mod(sm_ref, 0, is_ctx, _colsum(dh))
        _acc_mod(sm_ref, 2, is_ctx, _colsum(dh * (xr * w)))
        dn = dh * (1.0 + _mod_row(mc_ref, mx_ref, 1, is_ctx))
        _acc_row(sm_ref, 4, _colsum(dn * xr))
        dzo_ref[...] = dz_ref[...] + _rms_bwd(dn * w, x, r)

    return pl.pallas_call(
        body, name="pre_bwd", grid=(t // TR,),
        in_specs=[_row(D)] * 2 + _z_specs() + [_full((1, D)), _full((8, D)), _full((8, D))],
        out_specs=[pl.BlockSpec((TR, D), lambda i: (jnp.maximum(i - NCT, 0), 0)), _full((8, D))],
        out_shape=[jax.ShapeDtypeStruct((t - CTX, D), F32), jax.ShapeDtypeStruct((8, D), F32)],
        compiler_params=_cp("arbitrary"),
    )(dh1, dz, *z, nw, modc, modx)


def _merge_bwd(dm, p, u1, u2):
    t = dm.shape[0]

    def body(dm_ref, a0, a1, a2, b0, b1, b2, u1_ref, u2_ref, du1_ref, du2_ref, dg_ref):
        dm_ = dm_ref[...].astype(F32)
        s1 = _sig(_gate_window((a0, a1, a2)))
        s2 = _sig(_gate_window((b0, b1, b2)))
        du1_ref[...] = (dm_ * s1).astype(BF16)
        du2_ref[...] = (dm_ * s2).astype(BF16)
        dg_ref[:, :GW] = (dm_ * u1_ref[...].astype(F32) * s1 * (1.0 - s1)).astype(BF16)
        dg_ref[:, GW:] = (dm_ * u2_ref[...].astype(F32) * s2 * (1.0 - s2)).astype(BF16)

    return pl.pallas_call(
        body, name="merge_bwd", grid=(t // TR,),
        in_specs=[_row(GW)] + _gate_window_specs(GATE_HG0) + _gate_window_specs(GATE_GLA0) + [_row(GW), _row(GW)],
        out_specs=[_row(GW), _row(GW), _row(2 * GW)],
        out_shape=[jax.ShapeDtypeStruct((t, GW), BF16), jax.ShapeDtypeStruct((t, GW), BF16),
                   jax.ShapeDtypeStruct((t, 2 * GW), BF16)],
        compiler_params=_cp("parallel"),
    )(dm, p, p, p, p, p, p, u1, u2)


def _post_bwd(dy_hg, dy_gla, o_fw, o_bw, p, onw):
    t = o_fw.shape[0]

    def body(d1_ref, d2_ref, of_ref, ob_ref, g1_ref, g2_ref, w_ref, do_ref, dg_ref, sm_ref):
        @pl.when(pl.program_id(0) == 0)
        def _():
            sm_ref[...] = jnp.zeros_like(sm_ref)

        for h in range(NH):
            sl = slice(h * DH, (h + 1) * DH)
            gs = slice((h % (NH // 2)) * DH, (h % (NH // 2) + 1) * DH)
            g_ref, d_ref = (g1_ref, d1_ref) if h < NH // 2 else (g2_ref, d2_ref)
            o = of_ref[:, sl] + ob_ref[:, sl]
            r = _rstd(o)
            orr = o * r
            w = w_ref[:, sl]
            gt = g_ref[:, gs].astype(F32)
            dy = d_ref[:, gs].astype(F32)
            dg_ref[:, sl] = (dy * (orr * w) * _dsilu(gt)).astype(BF16)
            dn = dy * _silu(gt)
            sm_ref[0:1, sl] += _colsum(dn * orr)
            do_ref[:, sl] = _rms_bwd(dn * w, o, r)

    return pl.pallas_call(
        body, name="post_bwd", grid=(t // TR,),
        in_specs=[_row(HW), _row(HW), _row(D), _row(D), _rowcol(HW, MAIN0 // HW + 4), _rowcol(HW, MAIN0 // HW + 8),
                  _full((1, D))],
        out_specs=[_row(D), _row(D), _full((8, D))],
        out_shape=[jax.ShapeDtypeStruct((t, D), F32), jax.ShapeDtypeStruct((t, D), BF16),
                   jax.ShapeDtypeStruct((8, D), F32)],
        compiler_params=_cp("arbitrary"),
    )(dy_hg, dy_gla, o_fw, o_bw, p, p, onw)


def _gates_bwd(p, hg_lb, wgk, bgk, dgm, dgo, dq_f, dq_b, dv_f, dv_b, dk_f, dk_b, dg_f, dg_b):
    t = p.shape[0]
    seg = lambda j: _rowcol(HW, MAIN0 // HW + j)

    def body(hq_ref, hf_ref, hb_ref, lr_ref, lb_ref, wgk_ref, bgk_ref, dgm_ref, dgo_ref,
             dqf_ref, dqb_ref, dvf_ref, dvb_ref, dkf_ref, dkb_ref, dgf_ref, dgb_ref,
             dp_ref, dlb_ref, dw_ref, db_ref):
        @pl.when(pl.program_id(0) == 0)
        def _():
            dlb_ref[...] = jnp.zeros_like(dlb_ref)
            dw_ref[...] = jnp.zeros_like(dw_ref)
            db_ref[...] = jnp.zeros_like(db_ref)

        c0 = MAIN0

        def put(j, val):
            dp_ref[:, c0 + j * HW:c0 + (j + 1) * HW] = val.astype(BF16)

        dq = dqf_ref[...].astype(F32) + dqb_ref[...].astype(F32)
        dv = dvf_ref[...].astype(F32) + dvb_ref[...].astype(F32)
        put(0, dq[:, :HW] * _dsilu(hq_ref[...].astype(F32)))
        put(1, dv[:, :HW])
        put(5, dq[:, HW:] * (DH ** -0.5))
        put(7, dv[:, HW:])
        put(6, dkf_ref[:, HW:].astype(F32) + dkb_ref[:, HW:].astype(F32))
        dp_ref[:, c0 + 4 * HW:c0 + 5 * HW] = dgo_ref[:, :HW]
        dp_ref[:, c0 + 8 * HW:c0 + 9 * HW] = dgo_ref[:, HW:]
        lr = lr_ref[...].astype(BF16)
        xg = _dot(lr, wgk_ref[...], NN) + bgk_ref[...]
        dxg = []
        for d, (raw_ref, dk_ref, dg_ref) in enumerate(((hf_ref, dkf_ref, dgf_ref), (hb_ref, dkb_ref, dgb_ref))):
            lbd = _hg_lb(lb_ref, d)
            s = _sig(raw_ref[...].astype(F32))
            f = lbd + (1.0 - lbd) * s
            df = dg_ref[:, :HW] / f - dk_ref[:, :HW].astype(F32)
            put(2 + d, df * (1.0 - lbd) * s * (1.0 - s))
            dlb_ref[d:d + 1, :] += _colsum(df * (1.0 - s)) * (lbd * (1.0 - lbd))
            dxg.append(dg_ref[:, HW:] * (1.0 / GLA_NORM) * _sig(-xg[:, d * HW:(d + 1) * HW]))
        dxg = jnp.concatenate(dxg, axis=1)
        db_ref[0:1, :] += _colsum(dxg)
        dxg_b = dxg.astype(BF16)
        dw_ref[...] += _dot(lr, dxg_b, TN)
        dlr = _dot(dxg_b, wgk_ref[...], NT)
        dp_ref[:, LR0:LR0 + DH] = (dlr + dgm_ref[:, :DH].astype(F32)).astype(BF16)
        dp_ref[:, LR0 + DH:GATE_GLA0] = dgm_ref[:, DH:D]
        dp_ref[:, GATE_GLA0:GATE_GLA0 + DH] = dgm_ref[:, D:GW] + dgm_ref[:, GW:GW + DH]
        dp_ref[:, GATE_GLA0 + DH:GATE_GLA0 + GW] = dgm_ref[:, GW + DH:]
        dp_ref[:, GATE_GLA0 + GW:] = jnp.zeros((TR, W_IN_COLS - GATE_GLA0 - GW), BF16)

    return pl.pallas_call(
        body, name="gates_bwd", grid=(t // TR,),
        in_specs=[seg(0), seg(2), seg(3), _rowcol(DH, LR0 // DH), _full((2, 2, HW)), _full((DH, D)), _full((1, D)),
                  _row(2 * GW), _row(D)] + [_row(D)] * 8,
        out_specs=[_row(W_IN_COLS), _full((8, HW)), _full((DH, D)), _full((8, D))],
        out_shape=[jax.ShapeDtypeStruct((t, W_IN_COLS), BF16), jax.ShapeDtypeStruct((8, HW), F32),
                   jax.ShapeDtypeStruct((DH, D), F32), jax.ShapeDtypeStruct((8, D), F32)],
        compiler_params=_cp("arbitrary"),
    )(p, p, p, p, hg_lb, wgk, bgk, dgm, dgo, dq_f, dq_b, dv_f, dv_b, dk_f, dk_b, dg_f, dg_b)


def _scan_consts(rev):
    r = lax.broadcasted_iota(jnp.int32, (CHUNK, CHUNK), 0)
    u = lax.broadcasted_iota(jnp.int32, (CHUNK, CHUNK), 1)
    rp = lax.broadcasted_iota(jnp.int32, (CHUNK, 1), 0)
    if rev:
        r, u, rp = CHUNK - 1 - r, CHUNK - 1 - u, CHUNK - 1 - rp
    tri = jnp.where(u <= r, 1.0, 0.0).astype(F32)
    tri_t = jnp.where(r <= u, 1.0, 0.0).astype(F32)
    lv = []
    for b in LEVELS:
        sh = b.bit_length() - 1
        pair = ((r >> sh) == (u >> sh) + 1) & (((u >> sh) & 1) == 0)
        pair_t = ((u >> sh) == (r >> sh) + 1) & (((r >> sh) & 1) == 0)
        tside = ((rp >> sh) & 1) == 1
        lv.append((pair, pair_t, tside, jnp.where(tside, 1.0, -1.0).astype(F32)))
    bd = LEVELS[-1].bit_length() - 1
    diag = ((r >> bd) == (u >> bd)) & (u <= r)
    diag_t = ((r >> bd) == (u >> bd)) & (r <= u)
    return tri, tri_t, lv, diag, diag_t


def _row_of(pos, rev):
    return CHUNK - 1 - pos if rev else pos


def _chunk_terms(cum, b_scr, consts, rev):
    _, _, lv, _, _ = consts
    terms = []
    for b, (_, _, _, sgn) in zip(LEVELS, lv):
        pieces = []
        for j in range(CHUNK // (2 * b)):
            row = _row_of(2 * b * j + b - 1, rev)
            pieces.append(jnp.broadcast_to(b_scr[row:row + 1, :], (2 * b, DH)))
        if rev:
            pieces = pieces[::-1]
        bnd = pieces[0] if len(pieces) == 1 else jnp.concatenate(pieces, axis=0)
        terms.append(jnp.exp((cum - bnd) * sgn))
    b = LEVELS[-1]
    pieces = []
    for j in range(CHUNK // b):
        if j == 0:
            pieces.append(jnp.zeros((b, DH), F32))
        else:
            row = _row_of(b * j - 1, rev)
            pieces.append(jnp.broadcast_to(b_scr[row:row + 1, :], (b, DH)))
    if rev:
        pieces = pieces[::-1]
    start = jnp.concatenate(pieces, axis=0)
    wq = jnp.exp(jnp.minimum(cum - start, 0.0))
    wk = jnp.exp(jnp.minimum(start - cum, EXP_CLAMP))
    terms.append((wq, wk))
    return terms


def _run_staged(units):
    live = list(units)
    while live:
        nxt = []
        for u in live:
            try:
                next(u)
                nxt.append(u)
            except StopIteration:
                pass
        live = nxt


SCAN_TB = 256
SCAN_CB = SCAN_TB // CHUNK


def _block_order(i, ntb, rev):
    nctx = CTX // SCAN_TB
    if not rev:
        return i
    return jnp.where(i < nctx, nctx - 1 - i, ntb - 1 - (i - nctx))


def _chunk_in_block(j, rev):
    return SCAN_CB - 1 - j if rev else j


def _scan_fwd(q, k, v, g, rev):
    t = q.shape[0]
    nc = t // CHUNK
    hpb = SCAN_HEADS_FWD

    def body(q_ref, k_ref, v_ref, g_ref, o_ref, st_ref, s_scr, b_scr):
        consts = _scan_consts(rev)
        _, _, lv, diag, _ = consts
        masks = [lvl[0] for lvl in lv] + [diag]

        @pl.when(pl.program_id(1) == 0)
        def _():
            s_scr[...] = jnp.zeros_like(s_scr)

        tri = consts[0]
        state = {hh: s_scr[hh] for hh in range(hpb)}

        def unit(hh, j):
            sl = slice(hh * DH, (hh + 1) * DH)
            c = _chunk_in_block(j, rev)
            rows = slice(c * CHUNK, (c + 1) * CHUNK)
            b_ref = b_scr.at[hh * SCAN_CB + j]
            qc, kc, vc, gc = q_ref[rows, sl], k_ref[rows, sl], v_ref[rows, sl], g_ref[rows, sl]
            cum = _split_dot(tri, gc)
            b_ref[...] = cum
            yield
            terms = _chunk_terms(cum, b_ref, consts, rev)
            qf, kf = qc.astype(F32), kc.astype(F32)
            xs = [(jnp.where(tside, qf, kf) * w).astype(BF16) for w, (_, _, tside, _) in zip(terms[:-1], lv)]
            qd, kd = (qf * terms[-1][0]).astype(BF16), (kf * terms[-1][1]).astype(BF16)
            tot = _colsum(gc)
            qe = (qf * jnp.exp(cum)).astype(BF16)
            ke = (kf * jnp.exp(tot - cum)).astype(BF16)
            vb = vc.astype(BF16)
            yield
            scs = [_dot(x, x, NT) for x in xs] + [_dot(qd, kd, NT)]
            kv = _dot(vb, ke, TN)
            yield
            a = jnp.zeros((CHUNK, CHUNK), F32)
            for sc, m in zip(scs, masks):
                a = a + jnp.where(m, sc, 0.0)
            o_intra = _dot(a.astype(BF16), vb, NN)
            yield
            st = state[hh]
            st_ref[hh, c] = st
            o_ref[rows, sl] = o_intra + _dot(qe, st.astype(BF16), NT)
            state[hh] = st * jnp.exp(tot) + kv
            yield

        _run_staged([unit(hh, j) for hh in range(hpb) for j in range(SCAN_CB)])
        for hh in range(hpb):
            s_scr[hh] = state[hh]

    ntb = t // SCAN_TB
    col = pl.BlockSpec((SCAN_TB, hpb * DH), lambda h, i: (_block_order(i, ntb, rev), h))
    return pl.pallas_call(
        body, name="scan_fwd_" + ("bw" if rev else "fw"), grid=(NH // hpb, ntb),
        in_specs=[col] * 4,
        out_specs=[col, pl.BlockSpec((hpb, SCAN_CB, DH, DH), lambda h, i: (h, _block_order(i, ntb, rev), 0, 0))],
        out_shape=[jax.ShapeDtypeStruct((t, D), F32), jax.ShapeDtypeStruct((NH, nc, DH, DH), F32)],
        scratch_shapes=[pltpu.VMEM((hpb, DH, DH), F32), pltpu.VMEM((hpb * SCAN_CB, CHUNK, DH), F32)],
        compiler_params=_cp("parallel", "arbitrary"),
    )(q, k, v, g)


def _scan_bwd(q, k, v, g, do, states, rev):
    t = q.shape[0]
    nc = t // CHUNK
    hpb = SCAN_HEADS_BWD

    def body(q_ref, k_ref, v_ref, g_ref, do_ref, st_ref, dq_ref, dk_ref, dv_ref, dg_ref, ds_scr, b_scr):
        consts = _scan_consts(rev)
        _, tri_t, lv, diag, diag_t = consts
        masks = [(lvl[0], lvl[1]) for lvl in lv] + [(diag, diag_t)]
        @pl.when(pl.program_id(1) == 0)
        def _():
            ds_scr[...] = jnp.zeros_like(ds_scr)

        tri = consts[0]
        dstate = {hh: ds_scr[hh] for hh in range(hpb)}

        def unit(hh, jj):
            sl = slice(hh * DH, (hh + 1) * DH)
            c = _chunk_in_block(SCAN_CB - 1 - jj, rev)
            rows = slice(c * CHUNK, (c + 1) * CHUNK)
            b_ref = b_scr.at[hh * SCAN_CB + jj]
            qc, kc, vc, gc = q_ref[rows, sl], k_ref[rows, sl], v_ref[rows, sl], g_ref[rows, sl]
            dob = do_ref[rows, sl].astype(BF16)
            vb = vc.astype(BF16)
            cum = _split_dot(tri, gc)
            b_ref[...] = cum
            da = _dot(dob, vb, NT)
            da_t = _dot(vb, dob, NT)
            yield
            terms = _chunk_terms(cum, b_ref, consts, rev)
            qf, kf = qc.astype(F32), kc.astype(F32)
            xs = [(jnp.where(tside, qf, kf) * w).astype(BF16) for w, (_, _, tside, _) in zip(terms[:-1], lv)]
            wqd, wkd = terms[-1]
            qdb, kdb = (qf * wqd).astype(BF16), (kf * wkd).astype(BF16)
            tot = _colsum(gc)
            e_tot = jnp.exp(tot)
            e_b = jnp.exp(cum)
            e_t = jnp.exp(tot - cum)
            qeb = (qf * e_b).astype(BF16)
            keb = (kf * e_t).astype(BF16)
            dsym = [(jnp.where(m, da, 0.0) + jnp.where(m_t, da_t, 0.0)).astype(BF16) for m, m_t in masks[:-1]]
            dad = (jnp.where(diag, da, 0.0).astype(BF16), jnp.where(diag_t, da_t, 0.0).astype(BF16))
            yield
            sym = [_dot(x, x, NT) for x in xs]
            dxs = [_dot(d, x, NN) for d, x in zip(dsym, xs)]
            at_d = _dot(kdb, qdb, NT)
            dqt_d = _dot(dad[0], kdb, NN)
            dkt_d = _dot(dad[1], qdb, NN)
            qd = _dot(dob, qeb, TN)
            yield
            a_t = jnp.where(diag_t, at_d, 0.0)
            dq = dqt_d * wqd
            dk = dkt_d * wkd
            db = dqt_d * qdb.astype(F32) - dkt_d * kdb.astype(F32)
            for s, dx, x, w, (_, m_t, tside, sgn) in zip(sym, dxs, xs, terms[:-1], lv):
                a_t = a_t + jnp.where(m_t, s, 0.0)
                dxw = dx * w
                dq = dq + jnp.where(tside, dxw, 0.0)
                dk = dk + jnp.where(tside, 0.0, dxw)
                db = db + (dx * x.astype(F32)) * sgn
            dv_intra = _dot(a_t.astype(BF16), dob, NN)
            st = st_ref[hh, c]
            stb = st.astype(BF16)
            dqe = _dot(dob, stb, NN)
            yield
            dst = dstate[hh]
            dstb = dst.astype(BF16)
            dstate[hh] = dst * e_tot + qd
            dv_ref[rows, sl] = (dv_intra + _dot(keb, dstb, NT)).astype(BF16)
            dke = _dot(vb, dstb, NN)
            yield
            qe = qeb.astype(F32)
            ke = keb.astype(F32)
            dq_ref[rows, sl] = (dq + dqe * e_b).astype(BF16)
            dk_ref[rows, sl] = (dk + dke * e_t).astype(BF16)
            db = db + dqe * qe - dke * ke
            dtot = _colsum(dstb.astype(F32) * stb.astype(F32)) * e_tot + _colsum(dke * ke)
            dg_ref[rows, sl] = _split_dot(tri_t, db) + dtot
            yield

        _run_staged([unit(hh, jj) for hh in range(hpb) for jj in range(SCAN_CB)])
        for hh in range(hpb):
            ds_scr[hh] = dstate[hh]

    ntb = t // SCAN_TB
    blk = lambda i: _block_order(ntb - 1 - i, ntb, rev)
    col = pl.BlockSpec((SCAN_TB, hpb * DH), lambda h, i: (blk(i), h))
    out = jax.ShapeDtypeStruct((t, D), F32)
    outb = jax.ShapeDtypeStruct((t, D), BF16)
    return pl.pallas_call(
        body, name="scan_bwd_" + ("bw" if rev else "fw"), grid=(NH // hpb, ntb),
        in_specs=[col] * 5 + [pl.BlockSpec((hpb, SCAN_CB, DH, DH), lambda h, i: (h, blk(i), 0, 0))],
        out_specs=[col] * 4,
        out_shape=[outb] * 3 + [out],
        scratch_shapes=[pltpu.VMEM((hpb, DH, DH), F32), pltpu.VMEM((hpb * SCAN_CB, CHUNK, DH), F32)],
        compiler_params=_cp("parallel", "arbitrary"),
    )(q, k, v, g, do, states)


W_IN_GRAD_CHUNKS = (("a", (0, 512)), ("b", (0, 128)), ("b", (128, 512)))
W_IN_REF = 6688


def _layout_w_in(w):
    return jnp.pad(w, ((0, 0), (0, W_IN_COLS - W_IN_REF)))


def _unlayout_w_in(d):
    return d[:, :W_IN_REF]


def _gate_cols(w):
    return jnp.pad(w, ((0, 0), (GOFF, GW - GOFF - D)))


def _gate_rows(w):
    return jnp.pad(w, ((GOFF, GW - GOFF - D), (0, 0)))


def _layout_wgk(w):
    r = w.shape[1]
    top = jnp.concatenate([w[0], jnp.zeros_like(w[0])], axis=1)
    bot = jnp.concatenate([jnp.zeros_like(w[1]), w[1]], axis=1)
    return jnp.concatenate([top, bot, jnp.zeros((DH - 2 * r, D), w.dtype)], axis=0)


def _unlayout_wgk(d, r=16):
    return jnp.stack([d[:r, :HW], d[r:2 * r, HW:]])


def _local_step(z, target, modc, modx, norms, onw, hg_lb, wgk, bgk, get_w_in, get_mix, get_ffn, send):
    n_pre1, n_post1, n_pre2, n_post2 = norms
    t = z[0].shape[0] + z[1].shape[0]
    tm = 1152 if t % 1152 == 0 else 256
    h1 = _prenorm(z, n_pre1, modc, modx, 0, 1, "prenorm1")
    w_in = get_w_in(h1)
    p = _matmul(h1, w_in, NN, BF16, "mm_in", t, 1024, D)
    q, v, k_f, k_b, g_f, g_b = _gates_fwd(p, hg_lb, wgk, bgk)
    o_f, st_f = _scan_fwd(q, k_f, v, g_f, False)
    o_b, st_b = _scan_fwd(q, k_b, v, g_b, True)
    y = _post_fwd(o_f, o_b, p, onw)
    w_br_hg, w_br_gla, w_out = get_mix(y)
    u1 = _matmul(y, w_br_hg, NN, BF16, "mm_br_hg", tm, GW, HW, a_off=0)
    u2 = _matmul(y, w_br_gla, NN, BF16, "mm_br_gla", tm, GW, HW, a_off=1)
    merged = _merge_fwd(p, u1, u2)
    y1 = _matmul(merged, w_out, NN, BF16, "mm_out", tm, 512, GW)
    z1, h2 = _mid_fwd(z, y1, n_post1, n_pre2, modc, modx)
    w_gu_t, w_down = get_ffn(h2)
    uv = _matmul(h2, w_gu_t, NT, BF16, "mm_gu", t, D_FF // 2, D)
    act = _swiglu_fwd(uv)
    y2 = _matmul(act, w_down, NN, BF16, "mm_down", t, 512, D_FF)
    dz, dy2, loss_vec, sm_final = _final(z1, y2, target, n_post2, modc, modx)
    dact = _matmul(dy2, w_down, NT, BF16, "mm_down_dx", t, D_FF // 2, D)
    d_w_down = _matmul(act, dy2, TN, BF16, "mm_down_dw", D_FF // 2, 1024, t)
    duv = _swiglu_bwd(uv, dact)
    dh2 = _matmul(duv, w_gu_t, NN, BF16, "mm_gu_dx", tm, 512, D_FF)
    d_w_gate_t = _matmul(duv, h2, TN, BF16, "mm_gate_dw", D_FF // 2, 1024, t, a_off=0, m_out=D_FF)
    d_w_up_t = _matmul(duv, h2, TN, BF16, "mm_up_dw", D_FF // 2, 1024, t, a_off=2, m_out=D_FF)
    dh2 = send(("w_down", "w_gate_t", "w_up_t"), (d_w_down, d_w_gate_t, d_w_up_t), dh2)
    dz, dy1, sm_mid = _mid_bwd(dh2, dz, z1, y1, n_post1, n_pre2, modc, modx)
    dmerged = _matmul(dy1, w_out, NT, BF16, "mm_out_dx", tm, GW, D)
    d_w_out = _matmul(merged, dy1, TN, BF16, "mm_out_dw", GW, 512, t)
    du1, du2, dgm = _merge_bwd(dmerged, p, u1, u2)
    dy_hg = _matmul(du1, w_br_hg, NT, BF16, "mm_br_hg_dx", tm, HW, GW)
    dy_gla = _matmul(du2, w_br_gla, NT, BF16, "mm_br_gla_dx", tm, HW, GW)
    d_w_br_hg = _matmul(y, du1, TN, BF16, "mm_br_hg_dw", HW, GW, t, a_off=0, m_out=HW)
    d_w_br_gla = _matmul(y, du2, TN, BF16, "mm_br_gla_dw", HW, GW, t, a_off=1, m_out=HW)
    dy_hg = send(("w_out", "w_br_hg", "w_br_gla"), (d_w_out, d_w_br_hg, d_w_br_gla), dy_hg)
    do, dgo, sm_post = _post_bwd(dy_hg, dy_gla, o_f, o_b, p, onw)
    dq_f, dk_f, dv_f, dg_f = _scan_bwd(q, k_f, v, g_f, do, st_f, False)
    dq_b, dk_b, dv_b, dg_b = _scan_bwd(q, k_b, v, g_b, do, st_b, True)
    dp, d_lb, d_wgk, d_bgk = _gates_bwd(p, hg_lb, wgk, bgk, dgm, dgo, dq_f, dq_b, dv_f, dv_b, dk_f, dk_b, dg_f, dg_b)
    d_w_in_a = _matmul(h1, dp, TN, BF16, "mm_in_dw_a", 512, 1024, t, a_off=0, m_out=D // 2)
    dp = send(("w_in_a",), (d_w_in_a,), dp)
    d_w_in_b = _matmul(h1, dp, TN, BF16, "mm_in_dw_b", 512, 1024, t, a_off=1, m_out=D // 2)
    dp = send(("w_in_b",), (d_w_in_b,), dp)
    dh1 = _matmul(dp, w_in, NT, BF16, "mm_in_dx", tm, 512, W_IN_COLS // 2)
    grad_x, sm_pre = _pre_bwd(dh1, dz, z, n_pre1, modc, modx)
    return dict(loss_vec=loss_vec, grad_x=grad_x, sm_final=sm_final, sm_mid=sm_mid, sm_post=sm_post, sm_pre=sm_pre,
                d_lb=d_lb, d_wgk=d_wgk, d_bgk=d_bgk)


MESH = pl.DeviceIdType.MESH
ANY = pl.BlockSpec(memory_space=pl.ANY)
N_REL = N_DEV - 1


def _place():
    return lax.axis_index("x"), lax.axis_index("y"), lax.axis_index("c")


def _slot(p):
    return 4 * p[0] + 2 * p[1] + p[2]


def _all_gather(arrays, name):
    n = len(arrays)

    def body(*refs):
        ins, outs = refs[:n], refs[n:2 * n]
        send_sems, recv_sems, local_sems = refs[2 * n:]
        x, y, c = _place()
        me, sibling = (x, y, c), (x, y, 1 - c)
        chips = [(1 - x, y), (x, 1 - y), (1 - x, 1 - y)]

        def copy(a, k, block, to, src=None):
            dst = outs[a].at[_slot(block)]
            return pltpu.make_async_remote_copy(
                src_ref=dst if src is None else src, dst_ref=dst,
                send_sem=send_sems.at[N_REL * a + k], recv_sem=recv_sems.at[N_REL * a + k],
                device_id=to, device_id_type=MESH)

        mine = [pltpu.make_async_copy(ins[a], outs[a].at[_slot(me)], local_sems.at[a]) for a in range(n)]
        for cp in mine:
            cp.start()
        first = []
        for a in range(n):
            first.append(copy(a, 0, me, sibling, src=ins[a]))
            first += [copy(a, 1 + j, me, (*chip, c), src=ins[a]) for j, chip in enumerate(chips)]
        for cp in first:
            cp.start()
        passed = []
        for j, chip in enumerate(chips):
            for a in range(n):
                copy(a, 1 + j, (*chip, c), me).wait_recv()
                fwd = copy(a, 4 + j, (*chip, c), sibling)
                fwd.start()
                passed.append(fwd)
        for a in range(n):
            copy(a, 0, sibling, me).wait_recv()
        for j, chip in enumerate(chips):
            for a in range(n):
                copy(a, 4 + j, (*chip, 1 - c), me).wait_recv()
        for cp in first + passed:
            cp.wait_send()
        for cp in mine:
            cp.wait()

    return pl.pallas_call(
        body, name=name,
        in_specs=[ANY] * n, out_specs=[ANY] * n,
        out_shape=[jax.ShapeDtypeStruct((N_DEV,) + a.shape, a.dtype) for a in arrays],
        scratch_shapes=[pltpu.SemaphoreType.DMA((N_REL * n,)), pltpu.SemaphoreType.DMA((N_REL * n,)),
                        pltpu.SemaphoreType.DMA((n,))],
    )(*arrays)


def _exchange(arrays, name):
    n = len(arrays)

    def body(*refs):
        ins, outs = refs[:n], refs[n:2 * n]
        send_sems, recv_sems, local_sems = refs[2 * n:]
        x, y, c = _place()
        me = _slot((x, y, c))
        mine = [pltpu.make_async_copy(ins[a].at[me], outs[a].at[me], local_sems.at[a]) for a in range(n)]
        for cp in mine:
            cp.start()
        copies = []
        for a in range(n):
            for k in range(1, N_DEV):
                flip = lambda v, bit: 1 - v if bit else v
                peer = (flip(x, k & 4), flip(y, k & 2), flip(c, k & 1))
                copies.append(pltpu.make_async_remote_copy(
                    src_ref=ins[a].at[_slot(peer)], dst_ref=outs[a].at[me],
                    send_sem=send_sems.at[N_REL * a + k - 1], recv_sem=recv_sems.at[N_REL * a + k - 1],
                    device_id=peer, device_id_type=MESH))
                copies[-1].start()
        i = 0
        for a in range(n):
            for k in range(1, N_DEV):
                flip = lambda v, bit: 1 - v if bit else v
                peer = (flip(x, k & 4), flip(y, k & 2), flip(c, k & 1))
                pltpu.make_async_remote_copy(
                    src_ref=ins[a].at[_slot(peer)], dst_ref=outs[a].at[_slot(peer)],
                    send_sem=send_sems.at[N_REL * a + k - 1], recv_sem=recv_sems.at[N_REL * a + k - 1],
                    device_id=peer, device_id_type=MESH).wait_recv()
                i += 1
        for cp in copies:
            cp.wait_send()
        for cp in mine:
            cp.wait()

    return pl.pallas_call(
        body, name=name,
        in_specs=[ANY] * n, out_specs=[ANY] * n,
        out_shape=[jax.ShapeDtypeStruct(a.shape, a.dtype) for a in arrays],
        scratch_shapes=[pltpu.SemaphoreType.DMA((N_REL * n,)), pltpu.SemaphoreType.DMA((N_REL * n,)),
                        pltpu.SemaphoreType.DMA((n,))],
    )(*arrays)


HBM = pl.BlockSpec(memory_space=pltpu.HBM)
SEM = pl.BlockSpec(memory_space=pltpu.SEMAPHORE)
EFFECT = pltpu.SideEffectType.DATAFLOW_SIDE_EFFECTING


def _peer_of(x, y, c, k):
    flip = lambda v, bit: 1 - v if bit else v
    return flip(x, k & 4), flip(y, k & 2), flip(c, k & 1)


def _view_whole(src, slot):
    return src


def _view_near(src, slot):
    return src


_view_near.peers = (1, 2, 4, 6)


def _view_block(src, slot):
    return src.at[slot]


W_IN_SHARD = W_IN_REF // N_DEV


def _view_window(rows):
    def view(src, slot):
        col0 = pl.multiple_of((W_IN_SHARD * slot // DH) * DH, DH)
        return src.at[pl.ds(rows[0], rows[1] - rows[0]), pl.ds(col0, D)]
    return view


def _split_copies(view, srcs, lands, send_sems, recv_sems, local_sems):
    x, y, c = _place()
    me = _slot((x, y, c))
    local, sends, waits = [], [], []
    for a, (src, land) in enumerate(zip(srcs, lands)):
        local.append(pltpu.make_async_copy(view(src, me), land.at[me], local_sems.at[a]))
        for k in getattr(view, "peers", range(1, N_DEV)):
            peer = _peer_of(x, y, c, k)
            mine = view(src, _slot(peer))
            sems = dict(send_sem=send_sems.at[N_REL * a + k - 1], recv_sem=recv_sems.at[N_REL * a + k - 1],
                        device_id=peer, device_id_type=MESH)
            sends.append(pltpu.make_async_remote_copy(src_ref=mine, dst_ref=land.at[me], **sems))
            waits.append(pltpu.make_async_remote_copy(src_ref=mine, dst_ref=land.at[_slot(peer)], **sems))
    return local, sends, waits


def _split_start(view, land_shapes, srcs, name, after):
    n = len(srcs)
    lands = [lax.empty(shp, s.dtype) for shp, s in zip(land_shapes, srcs)]

    def body(*refs):
        src_refs, land_refs = refs[:n], refs[n:2 * n]
        send_sems, recv_sems, local_sems = refs[2 * n + 1:2 * n + 4]
        token = refs[-1]
        local, sends, _ = _split_copies(view, src_refs, land_refs, send_sems, recv_sems, local_sems)
        for cp in local + sends:
            cp.start()
        token[...] = jnp.zeros_like(token)

    hbm = lambda a: pltpu.with_memory_space_constraint(a, pltpu.HBM)
    out = pl.pallas_call(
        body, name=name,
        out_shape=(pltpu.SemaphoreType.DMA((N_REL * n,)), pltpu.SemaphoreType.DMA((N_REL * n,)),
                   pltpu.SemaphoreType.DMA((n,)),
                   *[pltpu.HBM(s.shape, s.dtype) for s in srcs], *[pltpu.HBM(l.shape, l.dtype) for l in lands],
                   jax.ShapeDtypeStruct((8, DH), F32)),
        in_specs=[HBM] * (2 * n) + [ANY],
        out_specs=(SEM, SEM, SEM, *([HBM] * (2 * n)), pl.BlockSpec(memory_space=pltpu.VMEM)),
        input_output_aliases={i: 3 + i for i in range(2 * n)},
        compiler_params=pltpu.CompilerParams(has_side_effects=EFFECT),
    )(*[hbm(s) for s in srcs], *[hbm(l) for l in lands], after)
    handle = dict(view=view, n=n, sems=out[:3], srcs=list(out[3:3 + n]), lands=list(out[3 + n:3 + 2 * n]))
    return handle, out[-1]


def _split_wait(handle, name, after, srcs=None):
    view, n, sems, lands = handle["view"], handle["n"], handle["sems"], handle["lands"]
    srcs = handle["srcs"] if srcs is None else srcs
    afters = list(after) if isinstance(after, (list, tuple)) else [after]

    def body(*refs):
        src_refs, land_refs = refs[:n], refs[n:2 * n]
        send_sems, recv_sems, local_sems = refs[2 * n:2 * n + 3]
        local, _, waits = _split_copies(view, src_refs, land_refs, send_sems, recv_sems, local_sems)
        for cp in waits:
            cp.wait_send()
            cp.wait_recv()
        for cp in local:
            cp.wait()

    out = pl.pallas_call(
        body, name=name,
        out_shape=(*[pltpu.HBM(s.shape, s.dtype) for s in srcs], *[pltpu.HBM(l.shape, l.dtype) for l in lands]),
        in_specs=[HBM] * (2 * n) + [SEM, SEM, SEM] + [ANY] * len(afters),
        out_specs=tuple([HBM] * (2 * n)),
        input_output_aliases={i: i for i in range(2 * n)},
        compiler_params=pltpu.CompilerParams(has_side_effects=EFFECT),
    )(*srcs, *lands, *sems, *afters)
    handle["srcs"] = list(out[:n])
    return list(out[n:])


def _tie(x, token, name):
    def body(x_ref, t_ref, o_ref):
        pass

    return pl.pallas_call(
        body, name=name, out_shape=jax.ShapeDtypeStruct(x.shape, x.dtype),
        in_specs=[ANY, ANY], out_specs=ANY, input_output_aliases={0: 0},
    )(x, token)


def _forward_to_sibling(land, name):
    def body(land_ref, out_ref, send_sems, recv_sems):
        x, y, c = _place()
        sibling = (x, y, 1 - c)
        chips = [(1 - x, y), (x, 1 - y), (1 - x, 1 - y)]

        def copy(j, core):
            blk = _slot((*chips[j], core))
            return pltpu.make_async_remote_copy(src_ref=land_ref.at[blk], dst_ref=out_ref.at[blk],
                                                send_sem=send_sems.at[j], recv_sem=recv_sems.at[j],
                                                device_id=sibling, device_id_type=MESH)

        sends = [copy(j, c) for j in range(3)]
        for cp in sends:
            cp.start()
        for j in range(3):
            copy(j, 1 - c).wait_recv()
        for cp in sends:
            cp.wait_send()

    return pl.pallas_call(
        body, name=name, in_specs=[ANY], out_specs=ANY, input_output_aliases={0: 0},
        out_shape=jax.ShapeDtypeStruct(land.shape, land.dtype),
        scratch_shapes=[pltpu.SemaphoreType.DMA((3,)), pltpu.SemaphoreType.DMA((3,))],
    )(land)


def _mod_fwd(a, w, b):
    def body(a_ref, w_ref, b_ref, o_ref):
        o_ref[...] = _dot(_silu(a_ref[...]), w_ref[...], NN, precision=HI) + b_ref[...]

    return pl.pallas_call(
        body, name="mod_fwd", out_shape=jax.ShapeDtypeStruct((a.shape[0], w.shape[1]), F32),
        compiler_params=pltpu.CompilerParams(vmem_limit_bytes=VMEM_LIMIT),
    )(a, w, b)


def _mod_bwd(a, d, w):
    def body(a_ref, d_ref, w_ref, dw_ref, dc_ref):
        av = a_ref[...]
        dv = d_ref[...]
        dw_ref[...] = _dot(_silu(av), dv, TN, precision=HI)
        da = _dot(dv[0:8, :], w_ref[...], NT, precision=HI) * _dsilu(av[0:8, :])
        row = lax.broadcasted_iota(jnp.int32, da.shape, 0)
        dc_ref[...] = jnp.where(row == 0, da, 0.0)

    return pl.pallas_call(
        body, name="mod_bwd",
        out_shape=[jax.ShapeDtypeStruct(w.shape, F32), jax.ShapeDtypeStruct((8, w.shape[0]), F32)],
        compiler_params=pltpu.CompilerParams(vmem_limit_bytes=VMEM_LIMIT),
    )(a, d, w)


def _sum_devices(g):
    def body(g_ref, o_ref):
        acc = g_ref[0]
        for i in range(1, g.shape[0]):
            acc = acc + g_ref[i]
        o_ref[...] = acc

    return pl.pallas_call(body, name="sum_devices_%d" % g.shape[1],
                          out_shape=jax.ShapeDtypeStruct(g.shape[1:], F32))(g)


def _sum_windows(g, name):
    n, r, c = g.shape
    tr = 128

    def body(g_ref, o_ref):
        x, y, cc = _place()
        lane0 = (W_IN_SHARD * _slot((x, y, cc))) % DH
        acc = g_ref[0].astype(F32)
        for i in range(1, n):
            acc = acc + g_ref[i].astype(F32)
        o_ref[...] = pltpu.roll(acc, (c - lane0) % c, 1).T

    return pl.pallas_call(
        body, name=name, grid=(r // tr,),
        in_specs=[pl.BlockSpec((n, tr, c), lambda i: (0, i, 0))],
        out_specs=pl.BlockSpec((c, tr), lambda i: (0, i)),
        out_shape=jax.ShapeDtypeStruct((c, r), F32),
        compiler_params=_cp("parallel"),
    )(g)


def _adam_rows(r, c, n):
    budget = 6 * 1024 * 1024
    best = None
    for tr in range(16, r + 1, 16):
        if r % tr == 0 and tr * c * (2 * n + 28) <= budget:
            best = tr
    return best if best is not None else r


def _adamw(g, w, m, v, name):
    n, r, c = g.shape
    tr = _adam_rows(r, c, n)
    bc1 = 1.0 - ADAM_B1 ** ADAM_STEP
    bc2 = 1.0 - ADAM_B2 ** ADAM_STEP

    def body(g_ref, w_ref, m_ref, v_ref, go_ref, d_ref, mo_ref, vo_ref):
        grad = g_ref[0].astype(F32)
        for i in range(1, n):
            grad = grad + g_ref[i].astype(F32)
        go_ref[...] = grad
        m_new = ADAM_B1 * m_ref[...] + (1.0 - ADAM_B1) * grad
        v_new = ADAM_B2 * v_ref[...] + (1.0 - ADAM_B2) * (grad * grad)
        mo_ref[...] = m_new
        vo_ref[...] = v_new
        d_ref[...] = -ADAM_LR * ((m_new / bc1) / (jnp.sqrt(v_new / bc2) + ADAM_EPS) + ADAM_WD * w_ref[...])

    blk = pl.BlockSpec((tr, c), lambda i: (i, 0))
    out = jax.ShapeDtypeStruct((r, c), F32)
    return pl.pallas_call(
        body, name=name, grid=(r // tr,),
        in_specs=[pl.BlockSpec((n, tr, c), lambda i: (0, i, 0)), blk, blk, blk],
        out_specs=[blk] * 4, out_shape=[out] * 4,
        compiler_params=_cp("parallel"),
    )(g, w, m, v)


def kernel(x, c, ctx, c_ctx, w_mod, b_mod, norm_pre1, norm_post1, norm_pre2, norm_post2, w_in, hg_lb, hg_onorm, gla_w_gk, gla_b_gk, gla_onorm, w_br_hg, w_br_gla, w_out, w_ff_gate, w_ff_up, w_ff_down, loss_target, m_c_ctx, m_w_mod, m_b_mod, m_norm_pre1, m_norm_post1, m_norm_pre2, m_norm_post2, m_w_in, m_hg_lb, m_hg_onorm, m_gla_w_gk, m_gla_b_gk, m_gla_onorm, m_w_br_hg, m_w_br_gla, m_w_out, m_w_ff_gate, m_w_ff_up, m_w_ff_down, v_c_ctx, v_w_mod, v_b_mod, v_norm_pre1, v_norm_post1, v_norm_pre2, v_norm_post2, v_w_in, v_hg_lb, v_hg_onorm, v_gla_w_gk, v_gla_b_gk, v_gla_onorm, v_w_br_hg, v_w_br_gla, v_w_out, v_w_ff_gate, v_w_ff_up, v_w_ff_down):
    xi, yi, ci = lax.axis_index("x"), lax.axis_index("y"), lax.axis_index("c")
    me = 4 * xi + 2 * yi + ci
    t = CTX + x.shape[1]

    c_all, lb_g, wgk_g, bgk_g = _all_gather([c, hg_lb, gla_w_gk[0], gla_b_gk[0]], "ag_small")
    tr_ = lambda a: jnp.swapaxes(a[0], 0, 1)
    big = [w_in[0], w_br_hg[0], w_br_gla[0], w_out[0], tr_(w_ff_gate), tr_(w_ff_up), w_ff_down[0]]
    big_bf = [w.astype(BF16) for w in big]
    cols = lambda g: jnp.transpose(g, (1, 0, 2)).reshape(g.shape[1], N_DEV * g.shape[2])

    def get_w_in(after):
        land, = _split_wait(w_in_handle, "ag_w_in_wait", after)
        return _layout_w_in(cols(_forward_to_sibling(land, "ag_w_in_forward")))

    def get_mix(after):
        g_brh, g_brg, g_out = _split_wait(mix_handle, "ag_mix_wait", after)
        return _gate_cols(cols(g_brh)), _gate_cols(cols(g_brg)), _gate_rows(g_out.reshape(D, D))

    def get_ffn(after):
        g_gate, g_up, g_down = _split_wait(ffn_handle, "ag_ffn_wait", after)
        return (g_gate.reshape(D_FF, D), g_up.reshape(D_FF, D)), g_down.reshape(D_FF, D)

    hg_lb_full = jnp.transpose(lb_g, (1, 2, 0, 3)).reshape(2, 2, HW)
    wgk_k = _layout_wgk(jnp.transpose(wgk_g, (1, 2, 0, 3)).reshape(2, 16, HW)).astype(BF16)
    bgk_k = jnp.transpose(bgk_g, (1, 0, 2)).reshape(1, D)
    onw = jnp.concatenate([jnp.tile(hg_onorm, (1, NH // 2)), jnp.tile(gla_onorm, (1, NH // 2))], axis=1)

    n_mod = w_mod.shape[2]
    a9 = jnp.concatenate([c_ctx[None], c_all[:, 0], jnp.zeros((16 - 1 - N_DEV, D), F32)], axis=0)
    b_loc = lax.dynamic_slice(b_mod, (0, me * n_mod), (1, n_mod))
    s_loc = _mod_fwd(a9, w_mod[0], b_loc)
    s_all, = _all_gather([s_loc], "ag_mod")
    mod_all = jnp.transpose(s_all, (1, 0, 2)).reshape(16, N_DEV * n_mod)
    pad8 = lambda m: jnp.concatenate([m.reshape(6, D), jnp.zeros((2, D), F32)], axis=0)
    modc = pad8(mod_all[0])
    modx = pad8(lax.dynamic_slice(mod_all, (1 + me, 0), (1, N_DEV * n_mod))[0])

    gathered = lambda arrs: [(N_DEV,) + a.shape for a in arrs]
    w_in_handle, tok = _split_start(_view_near, gathered(big_bf[:1]), big_bf[:1], "ag_w_in_start", s_all)
    mix_handle, tok = _split_start(_view_whole, gathered(big_bf[1:4]), big_bf[1:4], "ag_mix_start", tok)
    ffn_handle, tok = _split_start(_view_whole, gathered(big_bf[4:]), big_bf[4:], "ag_ffn_start", tok)

    z = (ctx[0], x[0])
    modx = _tie(modx, tok, "tie_mod")
    norms = (norm_pre1, norm_post1, norm_pre2, norm_post2)
    shard = lambda d: jnp.transpose(d.reshape(d.shape[0], N_DEV, -1), (1, 0, 2)).astype(BF16)
    rowshard = lambda d: d.reshape(N_DEV, d.shape[0] // N_DEV, d.shape[1]).astype(BF16)
    sent, w_in_grad = [], {}

    def send_w_in(i, x_after):
        half, rows = W_IN_GRAD_CHUNKS[i]
        handle, tok = _split_start(_view_window(rows), [(N_DEV, rows[1] - rows[0], D)], w_in_grad[half],
                                   "grads_w_in%d_start" % i, x_after)
        w_in_grad[half] = handle["srcs"]
        sent.append(("w_in%d" % i, ["w_in#%d" % i], handle))
        return _tie(x_after, tok, "tie_w_in%d" % i)

    def send(names, grads, x_after):
        if names == ("w_in_a",):
            w_in_grad["a"] = list(grads)
            return send_w_in(0, x_after)
        if names == ("w_in_b",):
            w_in_grad["b"] = list(grads)
            return x_after
        arrs, leaves = [], []
        for nm, g in zip(names, grads):
            if nm in ("w_gate_t", "w_up_t"):
                arrs.append(rowshard(g))
                leaves.append({"w_gate_t": "w_ff_gate", "w_up_t": "w_ff_up"}[nm])
            elif nm == "w_down":
                arrs.append(rowshard(g))
                leaves.append("w_ff_down")
            elif nm == "w_out":
                arrs.append(rowshard(g[GOFF:GOFF + D]))
                leaves.append(nm)
            else:
                arrs.append(shard(g[:, GOFF:GOFF + D]))
                leaves.append(nm)
        handle, tok = _split_start(_view_block, [a.shape for a in arrs], arrs, "grads_%s_start" % names[0], x_after)
        sent.append((names[0], leaves, handle))
        return _tie(x_after, tok, "tie_" + names[0])

    r = _local_step(z, loss_target[0], modc, modx, norms, onw, hg_lb_full, wgk_k, bgk_k,
                    get_w_in, get_mix, get_ffn, send)
    grad_x = r["grad_x"][None]

    sm_pre, sm_mid, sm_fin = r["sm_pre"], r["sm_mid"], r["sm_final"]
    dmodc = jnp.stack([sm_pre[0], sm_pre[2], sm_mid[4], sm_mid[0], sm_mid[2], sm_fin[0]]).reshape(-1)
    dmodx = jnp.stack([sm_pre[1], sm_pre[3], sm_mid[5], sm_mid[1], sm_mid[3], sm_fin[1]]).reshape(-1)
    on = r["sm_post"][0].reshape(NH, DH)
    pieces = [dmodc, dmodx, sm_pre[4], sm_mid[7], sm_mid[6], sm_fin[2], on[:NH // 2].sum(0), on[NH // 2:].sum(0),
              r["d_lb"][:2].reshape(-1), _unlayout_wgk(r["d_wgk"]).reshape(-1), r["d_bgk"][0]]
    loss_local = (0.5 / D) * jnp.sum(r["loss_vec"])
    pieces.append(jnp.concatenate([loss_local.reshape(1), jnp.zeros((DH - 1,), F32)]))
    sizes = [p.shape[0] for p in pieces]
    pack = jnp.concatenate(pieces).reshape(-1, DH)
    pack_all, = _all_gather([pack], "ag_small_grads")
    pack_all = send_w_in(1, pack_all)
    tot = _sum_devices(pack_all).reshape(-1)
    offs = [sum(sizes[:i]) for i in range(len(sizes))]
    part = lambda i: tot[offs[i]:offs[i] + sizes[i]]
    dmodc_t, dmodx_t = part(0), part(1)
    g_b_mod = (dmodc_t + dmodx_t)[None]
    g_norms = [part(i)[None] for i in (2, 3, 4, 5)]
    g_hg_on, g_gla_on = part(6)[None], part(7)[None]
    lb0 = lax.dynamic_slice(part(8).reshape(2, HW), (0, me * (HW // N_DEV)), (2, HW // N_DEV))
    g_hg_lb = jnp.stack([lb0, -lb0])
    g_wgk = lax.dynamic_slice(part(9).reshape(2, 16, HW), (0, 0, me * (HW // N_DEV)), (2, 16, HW // N_DEV))[None]
    g_bgk = lax.dynamic_slice(part(10).reshape(2, HW), (0, me * (HW // N_DEV)), (2, HW // N_DEV))[None]
    loss = part(11)[0]

    dmx_all = pack_all.reshape(N_DEV, -1)[:, sizes[0]:sizes[0] + sizes[1]]
    d9 = jnp.concatenate([lax.dynamic_slice(dmodc_t[None], (0, me * n_mod), (1, n_mod)),
                          lax.dynamic_slice(dmx_all, (0, me * n_mod), (N_DEV, n_mod)),
                          jnp.zeros((16 - 1 - N_DEV, n_mod), F32)], axis=0)
    g_w_mod, dcc_part = _mod_bwd(a9, d9, w_mod[0])
    dcc_all, = _all_gather([dcc_part], "ag_c_ctx")
    dcc_all = send_w_in(2, dcc_all)
    g_c_ctx = _sum_devices(dcc_all)[0]

    recv = {}
    for first, leaves, handle in sent:
        if not first.startswith("w_in"):
            recv.update(zip(leaves, _split_wait(handle, "grads_%s_wait" % first, g_c_ctx)))
    moms = [(m_w_in, v_w_in), (m_w_br_hg, v_w_br_hg), (m_w_br_gla, v_w_br_gla), (m_w_out, v_w_out),
            (m_w_ff_gate, v_w_ff_gate), (m_w_ff_up, v_w_ff_up), (m_w_ff_down, v_w_ff_down)]
    names = ["w_in", "w_br_hg", "w_br_gla", "w_out", "w_ff_gate", "w_ff_up", "w_ff_down"]
    res = {}

    def update(nm, w, m, v):
        if nm in ("w_ff_gate", "w_ff_up"):
            outs = _adamw(recv[nm], w, tr_(m), tr_(v), "adamw_" + nm)
            res[nm] = [jnp.swapaxes(o, 0, 1)[None] for o in outs]
        else:
            res[nm] = [o[None] for o in _adamw(recv[nm], w, m[0], v[0], "adamw_" + nm)]

    for nm, w, (m, v) in list(zip(names, big, moms))[1:]:
        update(nm, w, m, v)
    res["w_mod"] = [o[None] for o in _adamw(g_w_mod[None], w_mod[0], m_w_mod[0], v_w_mod[0], "adamw_w_mod")]

    small = [("c_ctx", c_ctx, m_c_ctx, v_c_ctx, g_c_ctx), ("b_mod", b_mod, m_b_mod, v_b_mod, g_b_mod),
             ("norm_pre1", norm_pre1, m_norm_pre1, v_norm_pre1, g_norms[0]),
             ("norm_post1", norm_post1, m_norm_post1, v_norm_post1, g_norms[1]),
             ("norm_pre2", norm_pre2, m_norm_pre2, v_norm_pre2, g_norms[2]),
             ("norm_post2", norm_post2, m_norm_post2, v_norm_post2, g_norms[3]),
             ("hg_lb", hg_lb, m_hg_lb, v_hg_lb, g_hg_lb), ("hg_onorm", hg_onorm, m_hg_onorm, v_hg_onorm, g_hg_on),
             ("gla_w_gk", gla_w_gk, m_gla_w_gk, v_gla_w_gk, g_wgk), ("gla_b_gk", gla_b_gk, m_gla_b_gk, v_gla_b_gk, g_bgk),
             ("gla_onorm", gla_onorm, m_gla_onorm, v_gla_onorm, g_gla_on)]
    flat = lambda k: jnp.concatenate([s[k].reshape(-1) for s in small]).reshape(-1, DH)
    outs = _adamw(flat(4)[None], flat(1), flat(2), flat(3), "adamw_small")
    off = 0
    for nm, w, _, _, _ in small:
        res[nm] = [o.reshape(-1)[off:off + w.size].reshape(w.shape) for o in outs]
        off += w.size

    done = [res[nm][0] for nm in names[1:]] + [res["w_mod"][0], outs[0]]
    sums = []
    for i, (first, leaves, handle) in enumerate(s for s in sent if s[0].startswith("w_in")):
        half = W_IN_GRAD_CHUNKS[i][0]
        land, = _split_wait(handle, "grads_%s_wait" % first, done, srcs=w_in_grad[half])
        w_in_grad[half] = handle["srcs"]
        sums.append(_sum_windows(land, "sum_windows%d" % i))
    g_t = jnp.concatenate(sums, axis=1)[:W_IN_SHARD]
    lin = lambda a: a.reshape(W_IN_SHARD * D // DH, DH)
    major = lambda a: lin(jnp.transpose(a, (2, 0, 1)))
    outs = _adamw(lin(g_t)[None], major(w_in), major(m_w_in), major(v_w_in), "adamw_w_in")
    res["w_in"] = [jnp.transpose(o.reshape(W_IN_SHARD, 1, D), (1, 2, 0)) for o in outs]

    order = ["c_ctx", "w_mod", "b_mod", "norm_pre1", "norm_post1", "norm_pre2", "norm_post2", "w_in", "hg_lb",
             "hg_onorm", "gla_w_gk", "gla_b_gk", "gla_onorm", "w_br_hg", "w_br_gla", "w_out", "w_ff_gate", "w_ff_up",
             "w_ff_down"]
    return (loss, grad_x, *[res[n][k] for k in range(4) for n in order])
```

```python
import functools

import jax
import jax.numpy as jnp
from jax import lax
from jax.experimental import pallas as pl
from jax.experimental.pallas import tpu as pltpu

F32 = jnp.float32
BF16 = jnp.bfloat16
HI = lax.Precision.HIGHEST

N_DEV = 8
D = 1024
CTX = 256
HW = 512
DH = 128
NH = 8
D_FF = 2816
EPS = 1e-6
GLA_NORM = 16.0
CHUNK = 64
TR = 256
NCT = CTX // TR
W_IN_COLS = 7168
MAIN0 = 0
LR0 = 4608
GW = 1152
GOFF = 32
GATE_HG0 = LR0
GATE_GLA0 = LR0 + D
LEVELS = (32, 16, 8)
EXP_CLAMP = 80.0
VMEM_LIMIT = 48 * 1024 * 1024

ADAM_LR, ADAM_B1, ADAM_B2, ADAM_EPS, ADAM_WD, ADAM_STEP = 0.001, 0.9, 0.999, 1e-08, 0.01, 10


def _cp(*sem):
    return pltpu.CompilerParams(dimension_semantics=sem, vmem_limit_bytes=VMEM_LIMIT)


def _sig(x):
    return jax.nn.sigmoid(x)


def _silu(x):
    return x * _sig(x)


def _dsilu(x):
    s = _sig(x)
    return s * (1.0 + x * (1.0 - s))


def _rstd(x):
    return lax.rsqrt(jnp.mean(x * x, axis=-1, keepdims=True) + EPS)


def _rms_bwd(a, y, r):
    return r * (a - y * (r * r) * jnp.mean(a * y, axis=-1, keepdims=True))


def _colsum(x):
    return jnp.sum(x, axis=0, keepdims=True)


def _dot(a, b, dims, precision=None):
    return lax.dot_general(a, b, (dims, ((), ())), preferred_element_type=F32, precision=precision)


NN = ((1,), (0,))
NT = ((1,), (1,))
TN = ((0,), (0,))

SCAN_HEADS_FWD = 4
SCAN_HEADS_BWD = 4


def _split_dot(m, x):
    mb = m.astype(BF16)
    x1 = x.astype(BF16)
    r1 = x - x1.astype(F32)
    x2 = r1.astype(BF16)
    x3 = (r1 - x2.astype(F32)).astype(BF16)
    return _dot(mb, x1, NN) + _dot(mb, x2, NN) + _dot(mb, x3, NN)


def _matmul(a, b, dims, out_dtype, name, tm, tn, tk, a_off=0, m_out=None):
    pair = isinstance(b, (tuple, list))
    bs = list(b) if pair else [b]
    b1 = bs[0]
    rows = b1.shape[0] * len(bs)
    half = None
    if dims == NN:
        m, k, n = a.shape[0], rows, b1.shape[1]
        a_spec = pl.BlockSpec((tm, tk), lambda i, j, kk: (i, kk + a_off))
        half = b1.shape[0] // tk
        b_maps = [lambda i, j, kk: (kk, j)] if not pair else [
            lambda i, j, kk: (jnp.minimum(kk, half - 1), j), lambda i, j, kk: (jnp.maximum(kk - half, 0), j)]
        b_specs = [pl.BlockSpec((tk, tn), f) for f in b_maps]
        axis = 2
    elif dims == NT:
        m, k, n = a.shape[0], b1.shape[1], rows
        a_spec = pl.BlockSpec((tm, tk), lambda i, j, kk: (i, kk + a_off))
        half = b1.shape[0] // tn
        b_maps = [lambda i, j, kk: (j, kk)] if not pair else [
            lambda i, j, kk: (jnp.minimum(j, half - 1), kk), lambda i, j, kk: (jnp.maximum(j - half, 0), kk)]
        b_specs = [pl.BlockSpec((tn, tk), f) for f in b_maps]
        axis = 1
    else:
        assert not pair
        m, k = (a.shape[1] if m_out is None else m_out), a.shape[0]
        n = b1.shape[1]
        a_spec = pl.BlockSpec((tk, tm), lambda i, j, kk: (kk, i + a_off))
        b_specs = [pl.BlockSpec((tk, tn), lambda i, j, kk: (kk, j))]
    assert m % tm == 0 and n % tn == 0 and k % tk == 0, (name, m, n, k, tm, tn, tk)
    nk = k // tk
    nb = len(bs)

    def body(a_ref, *refs):
        o_ref = refs[nb]
        if pair:
            bv = jnp.where(pl.program_id(axis) < half, refs[0][...], refs[1][...])
        else:
            bv = refs[0][...]
        part = _dot(a_ref[...], bv, dims)
        if nk == 1:
            o_ref[...] = part.astype(o_ref.dtype)
            return
        acc_ref = refs[nb + 1]
        kk = pl.program_id(2)

        @pl.when(kk == 0)
        def _():
            acc_ref[...] = part

        @pl.when(kk > 0)
        def _():
            acc_ref[...] += part

        @pl.when(kk == nk - 1)
        def _():
            o_ref[...] = acc_ref[...].astype(o_ref.dtype)

    return pl.pallas_call(
        body,
        name=name,
        grid=(m // tm, n // tn, nk),
        in_specs=[a_spec] + b_specs,
        out_specs=pl.BlockSpec((tm, tn), lambda i, j, kk: (i, j)),
        out_shape=jax.ShapeDtypeStruct((m, n), out_dtype),
        scratch_shapes=[] if nk == 1 else [pltpu.VMEM((tm, tn), F32)],
        compiler_params=_cp("parallel", "parallel", "arbitrary"),
    )(a, *bs)


def _row(c):
    return pl.BlockSpec((TR, c), lambda i: (i, 0))


def _rowcol(width, cb):
    return pl.BlockSpec((TR, width), lambda i: (i, cb))


def _full(shape):
    return pl.BlockSpec(shape, lambda i: (0,) * len(shape))


def _mod_row(mc_ref, mx_ref, k, is_ctx):
    return jnp.where(is_ctx, mc_ref[k:k + 1, :], mx_ref[k:k + 1, :])


def _z_specs():
    return [pl.BlockSpec((TR, D), lambda i: (jnp.minimum(i, NCT - 1), 0)),
            pl.BlockSpec((TR, D), lambda i: (jnp.maximum(i - NCT, 0), 0))]


def _z_tile(c_ref, x_ref, is_ctx):
    return jnp.where(is_ctx, c_ref[...], x_ref[...])


def _acc_row(ref, k, val):
    ref[k:k + 1, :] += val


def _acc_mod(ref, k, is_ctx, val):
    zero = jnp.zeros_like(val)
    ref[k:k + 1, :] += jnp.where(is_ctx, val, zero)
    ref[k + 1:k + 2, :] += jnp.where(is_ctx, zero, val)


def _prenorm(z, nw, modc, modx, i_shift, i_scale, name):
    t = z[0].shape[0] + z[1].shape[0]

    def body(zc_ref, zx_ref, nw_ref, mc_ref, mx_ref, h_ref):
        is_ctx = pl.program_id(0) < NCT
        x = _z_tile(zc_ref, zx_ref, is_ctx)
        n = x * _rstd(x) * nw_ref[...]
        h = n * (1.0 + _mod_row(mc_ref, mx_ref, i_scale, is_ctx)) + _mod_row(mc_ref, mx_ref, i_shift, is_ctx)
        h_ref[...] = h.astype(BF16)

    return pl.pallas_call(
        body, name=name, grid=(t // TR,),
        in_specs=_z_specs() + [_full((1, D)), _full((8, D)), _full((8, D))],
        out_specs=_row(D),
        out_shape=jax.ShapeDtypeStruct((t, D), BF16),
        compiler_params=_cp("parallel"),
    )(*z, nw, modc, modx)


def _hg_lb(lb_ref, d):
    a0 = lb_ref[0, d:d + 1, :]
    a1 = lb_ref[1, d:d + 1, :]
    mx = jnp.maximum(a0, a1)
    e0 = jnp.exp(a0 - mx)
    e1 = jnp.exp(a1 - mx)
    return e0 / (e0 + e1)


def _log_sigmoid(x):
    return jnp.minimum(x, 0.0) - jnp.log(1.0 + jnp.exp(-jnp.abs(x)))


def _gates_fwd(p, hg_lb, wgk, bgk):
    t = p.shape[0]
    seg = lambda j: _rowcol(HW, MAIN0 // HW + j)

    def body(hq_ref, hi_ref, hf_ref, hb_ref, gq_ref, gk_ref, gv_ref, lr_ref, lb_ref, wgk_ref, bgk_ref,
             q_ref, v_ref, kf_ref, kb_ref, gf_ref, gb_ref):
        q_ref[:, :HW] = _silu(hq_ref[...].astype(F32)).astype(BF16)
        q_ref[:, HW:] = (gq_ref[...].astype(F32) * (DH ** -0.5)).astype(BF16)
        v_ref[:, :HW] = hi_ref[...]
        v_ref[:, HW:] = gv_ref[...]
        xg = _dot(lr_ref[...].astype(BF16), wgk_ref[...], NN) + bgk_ref[...]
        for d, (raw_ref, k_ref, g_ref) in enumerate(((hf_ref, kf_ref, gf_ref), (hb_ref, kb_ref, gb_ref))):
            lbd = _hg_lb(lb_ref, d)
            f = lbd + (1.0 - lbd) * _sig(raw_ref[...].astype(F32))
            k_ref[:, :HW] = (1.0 - f).astype(BF16)
            k_ref[:, HW:] = gk_ref[...]
            g_ref[:, :HW] = jnp.log(f)
            g_ref[:, HW:] = _log_sigmoid(xg[:, d * HW:(d + 1) * HW]) * (1.0 / GLA_NORM)

    out = jax.ShapeDtypeStruct((t, D), F32)
    outb = jax.ShapeDtypeStruct((t, D), BF16)
    return pl.pallas_call(
        body, name="gates_fwd", grid=(t // TR,),
        in_specs=[seg(0), seg(1), seg(2), seg(3), seg(5), seg(6), seg(7), _rowcol(DH, LR0 // DH),
                  _full((2, 2, HW)), _full((DH, D)), _full((1, D))],
        out_specs=[_row(D)] * 6,
        out_shape=[outb] * 4 + [out] * 2,
        compiler_params=_cp("parallel"),
    )(p, p, p, p, p, p, p, p, hg_lb, wgk, bgk)


def _post_fwd(o_fw, o_bw, p, onw):
    t = o_fw.shape[0]

    def body(of_ref, ob_ref, g1_ref, g2_ref, w_ref, y_ref):
        for h in range(NH):
            sl = slice(h * DH, (h + 1) * DH)
            o = of_ref[:, sl] + ob_ref[:, sl]
            g_ref = g1_ref if h < NH // 2 else g2_ref
            gs = slice((h % (NH // 2)) * DH, (h % (NH // 2) + 1) * DH)
            n = o * _rstd(o) * w_ref[:, sl]
            y_ref[:, sl] = (n * _silu(g_ref[:, gs].astype(F32))).astype(BF16)

    return pl.pallas_call(
        body, name="post_fwd", grid=(t // TR,),
        in_specs=[_row(D), _row(D), _rowcol(HW, MAIN0 // HW + 4), _rowcol(HW, MAIN0 // HW + 8), _full((1, D))],
        out_specs=_row(D),
        out_shape=jax.ShapeDtypeStruct((t, D), BF16),
        compiler_params=_cp("parallel"),
    )(o_fw, o_bw, p, p, onw)


def _gate_window_specs(col0):
    return [_rowcol(HW, col0 // HW), _rowcol(HW, col0 // HW + 1), _rowcol(DH, (col0 + 2 * HW) // DH)]


def _gate_window(refs):
    return jnp.concatenate([r[...].astype(F32) for r in refs], axis=1)


def _merge_fwd(p, u1, u2):
    t = p.shape[0]

    def body(a0, a1, a2, b0, b1, b2, u1_ref, u2_ref, m_ref):
        f = lambda r: r[...].astype(F32)
        m_ref[...] = (_sig(_gate_window((a0, a1, a2))) * f(u1_ref)
                      + _sig(_gate_window((b0, b1, b2))) * f(u2_ref)).astype(BF16)

    return pl.pallas_call(
        body, name="merge_fwd", grid=(t // TR,),
        in_specs=_gate_window_specs(GATE_HG0) + _gate_window_specs(GATE_GLA0) + [_row(GW), _row(GW)],
        out_specs=_row(GW),
        out_shape=jax.ShapeDtypeStruct((t, GW), BF16),
        compiler_params=_cp("parallel"),
    )(p, p, p, p, p, p, u1, u2)


def _mid_fwd(z, y1, nw_post, nw_pre, modc, modx):
    t = y1.shape[0]

    def body(zc_ref, zx_ref, y_ref, wpo_ref, wpr_ref, mc_ref, mx_ref, z1_ref, h_ref):
        is_ctx = pl.program_id(0) < NCT
        y = y_ref[...].astype(F32)
        z1 = _z_tile(zc_ref, zx_ref, is_ctx) + _mod_row(mc_ref, mx_ref, 2, is_ctx) * (y * _rstd(y) * wpo_ref[...])
        z1_ref[...] = z1
        n = z1 * _rstd(z1) * wpr_ref[...]
        h = n * (1.0 + _mod_row(mc_ref, mx_ref, 4, is_ctx)) + _mod_row(mc_ref, mx_ref, 3, is_ctx)
        h_ref[...] = h.astype(BF16)

    return pl.pallas_call(
        body, name="mid_fwd", grid=(t // TR,),
        in_specs=_z_specs() + [_row(D), _full((1, D)), _full((1, D)), _full((8, D)), _full((8, D))],
        out_specs=[_row(D), _row(D)],
        out_shape=[jax.ShapeDtypeStruct((t, D), F32), jax.ShapeDtypeStruct((t, D), BF16)],
        compiler_params=_cp("parallel"),
    )(*z, y1, nw_post, nw_pre, modc, modx)


def _swiglu_fwd(uv):
    t = uv.shape[0]

    def body(u_ref, v_ref, a_ref):
        a_ref[...] = (_silu(u_ref[...].astype(F32)) * v_ref[...].astype(F32)).astype(BF16)

    return pl.pallas_call(
        body, name="swiglu_fwd", grid=(t // TR,),
        in_specs=[_rowcol(D_FF, 0), _rowcol(D_FF, 1)],
        out_specs=_row(D_FF),
        out_shape=jax.ShapeDtypeStruct((t, D_FF), BF16),
        compiler_params=_cp("parallel"),
    )(uv, uv)


def _swiglu_bwd(uv, da):
    t = uv.shape[0]

    def body(u_ref, v_ref, da_ref, d_ref):
        u = u_ref[...].astype(F32)
        d = da_ref[...].astype(F32)
        d_ref[:, :D_FF] = (d * v_ref[...].astype(F32) * _dsilu(u)).astype(BF16)
        d_ref[:, D_FF:] = (d * _silu(u)).astype(BF16)

    return pl.pallas_call(
        body, name="swiglu_bwd", grid=(t // TR,),
        in_specs=[_rowcol(D_FF, 0), _rowcol(D_FF, 1), _row(D_FF)],
        out_specs=_row(2 * D_FF),
        out_shape=jax.ShapeDtypeStruct((t, 2 * D_FF), BF16),
        compiler_params=_cp("parallel"),
    )(uv, uv, da)


def _final(z1, y2, target, nw, modc, modx):
    t = z1.shape[0]

    def body(z1_ref, y_ref, tg_ref, w_ref, mc_ref, mx_ref, dz_ref, dy_ref, loss_ref, sm_ref):
        i = pl.program_id(0)
        is_ctx = i < NCT

        @pl.when(i == 0)
        def _():
            loss_ref[...] = jnp.zeros_like(loss_ref)
            sm_ref[...] = jnp.zeros_like(sm_ref)

        g = _mod_row(mc_ref, mx_ref, 5, is_ctx)
        y = y_ref[...].astype(F32)
        r = _rstd(y)
        w = w_ref[...]
        yr = y * r
        n = yr * w
        e = z1_ref[...] + g * n - tg_ref[...]
        lat = jnp.where(is_ctx, 0.0, 1.0)
        loss_ref[...] += lat * _colsum(e * e)
        dz = e * (lat / D)
        dz_ref[...] = dz
        _acc_mod(sm_ref, 0, is_ctx, _colsum(dz * n))
        dn = dz * g
        _acc_row(sm_ref, 2, _colsum(dn * yr))
        dy_ref[...] = _rms_bwd(dn * w, y, r).astype(BF16)

    return pl.pallas_call(
        body, name="final", grid=(t // TR,),
        in_specs=[_row(D), _row(D), pl.BlockSpec((TR, D), lambda i: (jnp.maximum(i - NCT, 0), 0)),
                  _full((1, D)), _full((8, D)), _full((8, D))],
        out_specs=[_row(D), _row(D), _full((1, D)), _full((8, D))],
        out_shape=[jax.ShapeDtypeStruct((t, D), F32), jax.ShapeDtypeStruct((t, D), BF16),
                   jax.ShapeDtypeStruct((1, D), F32), jax.ShapeDtypeStruct((8, D), F32)],
        compiler_params=_cp("arbitrary"),
    )(z1, y2, target, nw, modc, modx)


def _mid_bwd(dh2, dz, z1, y1, nw_post, nw_pre, modc, modx):
    t = z1.shape[0]

    def body(dh_ref, dz_ref, z1_ref, y_ref, wpo_ref, wpr_ref, mc_ref, mx_ref, dzo_ref, dy_ref, sm_ref):
        i = pl.program_id(0)
        is_ctx = i < NCT

        @pl.when(i == 0)
        def _():
            sm_ref[...] = jnp.zeros_like(sm_ref)

        dh = dh_ref[...].astype(F32)
        z1 = z1_ref[...]
        r = _rstd(z1)
        zr = z1 * r
        wpr = wpr_ref[...]
        n = zr * wpr
        _acc_mod(sm_ref, 0, is_ctx, _colsum(dh))
        _acc_mod(sm_ref, 2, is_ctx, _colsum(dh * n))
        dn = dh * (1.0 + _mod_row(mc_ref, mx_ref, 4, is_ctx))
        _acc_row(sm_ref, 6, _colsum(dn * zr))
        dz1 = dz_ref[...] + _rms_bwd(dn * wpr, z1, r)
        dzo_ref[...] = dz1
        y = y_ref[...].astype(F32)
        r1 = _rstd(y)
        yr = y * r1
        wpo = wpo_ref[...]
        g = _mod_row(mc_ref, mx_ref, 2, is_ctx)
        _acc_mod(sm_ref, 4, is_ctx, _colsum(dz1 * (yr * wpo)))
        dn1 = dz1 * g
        _acc_row(sm_ref, 7, _colsum(dn1 * yr))
        dy_ref[...] = _rms_bwd(dn1 * wpo, y, r1).astype(BF16)

    return pl.pallas_call(
        body, name="mid_bwd", grid=(t // TR,),
        in_specs=[_row(D)] * 4 + [_full((1, D)), _full((1, D)), _full((8, D)), _full((8, D))],
        out_specs=[_row(D), _row(D), _full((8, D))],
        out_shape=[jax.ShapeDtypeStruct((t, D), F32), jax.ShapeDtypeStruct((t, D), BF16),
                   jax.ShapeDtypeStruct((8, D), F32)],
        compiler_params=_cp("arbitrary"),
    )(dh2, dz, z1, y1, nw_post, nw_pre, modc, modx)


def _pre_bwd(dh1, dz, z, nw, modc, modx):
    t = dh1.shape[0]

    def body(dh_ref, dz_ref, zc_ref, zx_ref, w_ref, mc_ref, mx_ref, dzo_ref, sm_ref):
        i = pl.program_id(0)
        is_ctx = i < NCT

        @pl.when(i == 0)
        def _():
            sm_ref[...] = jnp.zeros_like(sm_ref)

        dh = dh_ref[...].astype(F32)
        x = _z_tile(zc_ref, zx_ref, is_ctx)
        r = _rstd(x)
        xr = x * r
        w = w_ref[...]
        _acc_mod(sm_ref, 0, is_ctx, _colsum(dh))
        _acc_mod(sm_ref, 2, is_ctx, _colsum(dh * (xr * w)))
        dn = dh * (1.0 + _mod_row(mc_ref, mx_ref, 1, is_ctx))
        _acc_row(sm_ref, 4, _colsum(dn * xr))
        dzo_ref[...] = dz_ref[...] + _rms_bwd(dn * w, x, r)

    return pl.pallas_call(
        body, name="pre_bwd", grid=(t // TR,),
        in_specs=[_row(D)] * 2 + _z_specs() + [_full((1, D)), _full((8, D)), _full((8, D))],
        out_specs=[pl.BlockSpec((TR, D), lambda i: (jnp.maximum(i - NCT, 0), 0)), _full((8, D))],
        out_shape=[jax.ShapeDtypeStruct((t - CTX, D), F32), jax.ShapeDtypeStruct((8, D), F32)],
        compiler_params=_cp("arbitrary"),
    )(dh1, dz, *z, nw, modc, modx)


def _merge_bwd(dm, p, u1, u2):
    t = dm.shape[0]

    def body(dm_ref, a0, a1, a2, b0, b1, b2, u1_ref, u2_ref, du1_ref, du2_ref, dg_ref):
        dm_ = dm_ref[...].astype(F32)
        s1 = _sig(_gate_window((a0, a1, a2)))
        s2 = _sig(_gate_window((b0, b1, b2)))
        du1_ref[...] = (dm_ * s1).astype(BF16)
        du2_ref[...] = (dm_ * s2).astype(BF16)
        dg_ref[:, :GW] = (dm_ * u1_ref[...].astype(F32) * s1 * (1.0 - s1)).astype(BF16)
        dg_ref[:, GW:] = (dm_ * u2_ref[...].astype(F32) * s2 * (1.0 - s2)).astype(BF16)

    return pl.pallas_call(
        body, name="merge_bwd", grid=(t // TR,),
        in_specs=[_row(GW)] + _gate_window_specs(GATE_HG0) + _gate_window_specs(GATE_GLA0) + [_row(GW), _row(GW)],
        out_specs=[_row(GW), _row(GW), _row(2 * GW)],
        out_shape=[jax.ShapeDtypeStruct((t, GW), BF16), jax.ShapeDtypeStruct((t, GW), BF16),
                   jax.ShapeDtypeStruct((t, 2 * GW), BF16)],
        compiler_params=_cp("parallel"),
    )(dm, p, p, p, p, p, p, u1, u2)


def _post_bwd(dy_hg, dy_gla, o_fw, o_bw, p, onw):
    t = o_fw.shape[0]

    def body(d1_ref, d2_ref, of_ref, ob_ref, g1_ref, g2_ref, w_ref, do_ref, dg_ref, sm_ref):
        @pl.when(pl.program_id(0) == 0)
        def _():
            sm_ref[...] = jnp.zeros_like(sm_ref)

        for h in range(NH):
            sl = slice(h * DH, (h + 1) * DH)
            gs = slice((h % (NH // 2)) * DH, (h % (NH // 2) + 1) * DH)
            g_ref, d_ref = (g1_ref, d1_ref) if h < NH // 2 else (g2_ref, d2_ref)
            o = of_ref[:, sl] + ob_ref[:, sl]
            r = _rstd(o)
            orr = o * r
            w = w_ref[:, sl]
            gt = g_ref[:, gs].astype(F32)
            dy = d_ref[:, gs].astype(F32)
            dg_ref[:, sl] = (dy * (orr * w) * _dsilu(gt)).astype(BF16)
            dn = dy * _silu(gt)
            sm_ref[0:1, sl] += _colsum(dn * orr)
            do_ref[:, sl] = _rms_bwd(dn * w, o, r)

    return pl.pallas_call(
        body, name="post_bwd", grid=(t // TR,),
        in_specs=[_row(HW), _row(HW), _row(D), _row(D), _rowcol(HW, MAIN0 // HW + 4), _rowcol(HW, MAIN0 // HW + 8),
                  _full((1, D))],
        out_specs=[_row(D), _row(D), _full((8, D))],
        out_shape=[jax.ShapeDtypeStruct((t, D), F32), jax.ShapeDtypeStruct((t, D), BF16),
                   jax.ShapeDtypeStruct((8, D), F32)],
        compiler_params=_cp("arbitrary"),
    )(dy_hg, dy_gla, o_fw, o_bw, p, p, onw)


def _gates_bwd(p, hg_lb, wgk, bgk, dgm, dgo, dq_f, dq_b, dv_f, dv_b, dk_f, dk_b, dg_f, dg_b):
    t = p.shape[0]
    seg = lambda j: _rowcol(HW, MAIN0 // HW + j)

    def body(hq_ref, hf_ref, hb_ref, lr_ref, lb_ref, wgk_ref, bgk_ref, dgm_ref, dgo_ref,
             dqf_ref, dqb_ref, dvf_ref, dvb_ref, dkf_ref, dkb_ref, dgf_ref, dgb_ref,
             dp_ref, dlb_ref, dw_ref, db_ref):
        @pl.when(pl.program_id(0) == 0)
        def _():
            dlb_ref[...] = jnp.zeros_like(dlb_ref)
            dw_ref[...] = jnp.zeros_like(dw_ref)
            db_ref[...] = jnp.zeros_like(db_ref)

        c0 = MAIN0

        def put(j, val):
            dp_ref[:, c0 + j * HW:c0 + (j + 1) * HW] = val.astype(BF16)

        dq = dqf_ref[...].astype(F32) + dqb_ref[...].astype(F32)
        dv = dvf_ref[...].astype(F32) + dvb_ref[...].astype(F32)
        put(0, dq[:, :HW] * _dsilu(hq_ref[...].astype(F32)))
        put(1, dv[:, :HW])
        put(5, dq[:, HW:] * (DH ** -0.5))
        put(7, dv[:, HW:])
        put(6, dkf_ref[:, HW:].astype(F32) + dkb_ref[:, HW:].astype(F32))
        dp_ref[:, c0 + 4 * HW:c0 + 5 * HW] = dgo_ref[:, :HW]
        dp_ref[:, c0 + 8 * HW:c0 + 9 * HW] = dgo_ref[:, HW:]
        lr = lr_ref[...].astype(BF16)
        xg = _dot(lr, wgk_ref[...], NN) + bgk_ref[...]
        dxg = []
        for d, (raw_ref, dk_ref, dg_ref) in enumerate(((hf_ref, dkf_ref, dgf_ref), (hb_ref, dkb_ref, dgb_ref))):
            lbd = _hg_lb(lb_ref, d)
            s = _sig(raw_ref[...].astype(F32))
            f = lbd + (1.0 - lbd) * s
            df = dg_ref[:, :HW] / f - dk_ref[:, :HW].astype(F32)
            put(2 + d, df * (1.0 - lbd) * s * (1.0 - s))
            dlb_ref[d:d + 1, :] += _colsum(df * (1.0 - s)) * (lbd * (1.0 - lbd))
            dxg.append(dg_ref[:, HW:] * (1.0 / GLA_NORM) * _sig(-xg[:, d * HW:(d + 1) * HW]))
        dxg = jnp.concatenate(dxg, axis=1)
        db_ref[0:1, :] += _colsum(dxg)
        dxg_b = dxg.astype(BF16)
        dw_ref[...] += _dot(lr, dxg_b, TN)
        dlr = _dot(dxg_b, wgk_ref[...], NT)
        dp_ref[:, LR0:LR0 + DH] = (dlr + dgm_ref[:, :DH].astype(F32)).astype(BF16)
        dp_ref[:, LR0 + DH:GATE_GLA0] = dgm_ref[:, DH:D]
        dp_ref[:, GATE_GLA0:GATE_GLA0 + DH] = dgm_ref[:, D:GW] + dgm_ref[:, GW:GW + DH]
        dp_ref[:, GATE_GLA0 + DH:GATE_GLA0 + GW] = dgm_ref[:, GW + DH:]
        dp_ref[:, GATE_GLA0 + GW:] = jnp.zeros((TR, W_IN_COLS - GATE_GLA0 - GW), BF16)

    return pl.pallas_call(
        body, name="gates_bwd", grid=(t // TR,),
        in_specs=[seg(0), seg(2), seg(3), _rowcol(DH, LR0 // DH), _full((2, 2, HW)), _full((DH, D)), _full((1, D)),
                  _row(2 * GW), _row(D)] + [_row(D)] * 8,
        out_specs=[_row(W_IN_COLS), _full((8, HW)), _full((DH, D)), _full((8, D))],
        out_shape=[jax.ShapeDtypeStruct((t, W_IN_COLS), BF16), jax.ShapeDtypeStruct((8, HW), F32),
                   jax.ShapeDtypeStruct((DH, D), F32), jax.ShapeDtypeStruct((8, D), F32)],
        compiler_params=_cp("arbitrary"),
    )(p, p, p, p, hg_lb, wgk, bgk, dgm, dgo, dq_f, dq_b, dv_f, dv_b, dk_f, dk_b, dg_f, dg_b)


def _scan_consts(rev):
    r = lax.broadcasted_iota(jnp.int32, (CHUNK, CHUNK), 0)
    u = lax.broadcasted_iota(jnp.int32, (CHUNK, CHUNK), 1)
    rp = lax.broadcasted_iota(jnp.int32, (CHUNK, 1), 0)
    if rev:
        r, u, rp = CHUNK - 1 - r, CHUNK - 1 - u, CHUNK - 1 - rp
    tri = jnp.where(u <= r, 1.0, 0.0).astype(F32)
    tri_t = jnp.where(r <= u, 1.0, 0.0).astype(F32)
    lv = []
    for b in LEVELS:
        sh = b.bit_length() - 1
        pair = ((r >> sh) == (u >> sh) + 1) & (((u >> sh) & 1) == 0)
        pair_t = ((u >> sh) == (r >> sh) + 1) & (((r >> sh) & 1) == 0)
        tside = ((rp >> sh) & 1) == 1
        lv.append((pair, pair_t, tside, jnp.where(tside, 1.0, -1.0).astype(F32)))
    bd = LEVELS[-1].bit_length() - 1
    diag = ((r >> bd) == (u >> bd)) & (u <= r)
    diag_t = ((r >> bd) == (u >> bd)) & (r <= u)
    return tri, tri_t, lv, diag, diag_t


def _row_of(pos, rev):
    return CHUNK - 1 - pos if rev else pos


def _chunk_terms(cum, b_scr, consts, rev):
    _, _, lv, _, _ = consts
    terms = []
    for b, (_, _, _, sgn) in zip(LEVELS, lv):
        pieces = []
        for j in range(CHUNK // (2 * b)):
            row = _row_of(2 * b * j + b - 1, rev)
            pieces.append(jnp.broadcast_to(b_scr[row:row + 1, :], (2 * b, DH)))
        if rev:
            pieces = pieces[::-1]
        bnd = pieces[0] if len(pieces) == 1 else jnp.concatenate(pieces, axis=0)
        terms.append(jnp.exp((cum - bnd) * sgn))
    b = LEVELS[-1]
    pieces = []
    for j in range(CHUNK // b):
        if j == 0:
            pieces.append(jnp.zeros((b, DH), F32))
        else:
            row = _row_of(b * j - 1, rev)
            pieces.append(jnp.broadcast_to(b_scr[row:row + 1, :], (b, DH)))
    if rev:
        pieces = pieces[::-1]
    start = jnp.concatenate(pieces, axis=0)
    wq = jnp.exp(jnp.minimum(cum - start, 0.0))
    wk = jnp.exp(jnp.minimum(start - cum, EXP_CLAMP))
    terms.append((wq, wk))
    return terms


def _run_staged(units):
    live = list(units)
    while live:
        nxt = []
        for u in live:
            try:
                next(u)
                nxt.append(u)
            except StopIteration:
                pass
        live = nxt


SCAN_TB = 256
SCAN_CB = SCAN_TB // CHUNK


def _block_order(i, ntb, rev):
    nctx = CTX // SCAN_TB
    if not rev:
        return i
    return jnp.where(i < nctx, nctx - 1 - i, ntb - 1 - (i - nctx))


def _chunk_in_block(j, rev):
    return SCAN_CB - 1 - j if rev else j


def _scan_fwd(q, k, v, g, rev):
    t = q.shape[0]
    nc = t // CHUNK
    hpb = SCAN_HEADS_FWD

    def body(q_ref, k_ref, v_ref, g_ref, o_ref, st_ref, s_scr, b_scr):
        consts = _scan_consts(rev)
        _, _, lv, diag, _ = consts
        masks = [lvl[0] for lvl in lv] + [diag]

        @pl.when(pl.program_id(1) == 0)
        def _():
            s_scr[...] = jnp.zeros_like(s_scr)

        tri = consts[0]
        state = {hh: s_scr[hh] for hh in range(hpb)}

        def unit(hh, j):
            sl = slice(hh * DH, (hh + 1) * DH)
            c = _chunk_in_block(j, rev)
            rows = slice(c * CHUNK, (c + 1) * CHUNK)
            b_ref = b_scr.at[hh * SCAN_CB + j]
            qc, kc, vc, gc = q_ref[rows, sl], k_ref[rows, sl], v_ref[rows, sl], g_ref[rows, sl]
            cum = _split_dot(tri, gc)
            b_ref[...] = cum
            yield
            terms = _chunk_terms(cum, b_ref, consts, rev)
            qf, kf = qc.astype(F32), kc.astype(F32)
            xs = [(jnp.where(tside, qf, kf) * w).astype(BF16) for w, (_, _, tside, _) in zip(terms[:-1], lv)]
            qd, kd = (qf * terms[-1][0]).astype(BF16), (kf * terms[-1][1]).astype(BF16)
            tot = _colsum(gc)
            qe = (qf * jnp.exp(cum)).astype(BF16)
            ke = (kf * jnp.exp(tot - cum)).astype(BF16)
            vb = vc.astype(BF16)
            yield
            scs = [_dot(x, x, NT) for x in xs] + [_dot(qd, kd, NT)]
            kv = _dot(vb, ke, TN)
            yield
            a = jnp.zeros((CHUNK, CHUNK), F32)
            for sc, m in zip(scs, masks):
                a = a + jnp.where(m, sc, 0.0)
            o_intra = _dot(a.astype(BF16), vb, NN)
            yield
            st = state[hh]
            st_ref[hh, c] = st
            o_ref[rows, sl] = o_intra + _dot(qe, st.astype(BF16), NT)
            state[hh] = st * jnp.exp(tot) + kv
            yield

        _run_staged([unit(hh, j) for hh in range(hpb) for j in range(SCAN_CB)])
        for hh in range(hpb):
            s_scr[hh] = state[hh]

    ntb = t // SCAN_TB
    col = pl.BlockSpec((SCAN_TB, hpb * DH), lambda h, i: (_block_order(i, ntb, rev), h))
    return pl.pallas_call(
        body, name="scan_fwd_" + ("bw" if rev else "fw"), grid=(NH // hpb, ntb),
        in_specs=[col] * 4,
        out_specs=[col, pl.BlockSpec((hpb, SCAN_CB, DH, DH), lambda h, i: (h, _block_order(i, ntb, rev), 0, 0))],
        out_shape=[jax.ShapeDtypeStruct((t, D), F32), jax.ShapeDtypeStruct((NH, nc, DH, DH), F32)],
        scratch_shapes=[pltpu.VMEM((hpb, DH, DH), F32), pltpu.VMEM((hpb * SCAN_CB, CHUNK, DH), F32)],
        compiler_params=_cp("parallel", "arbitrary"),
    )(q, k, v, g)


def _scan_bwd(q, k, v, g, do, states, rev):
    t = q.shape[0]
    nc = t // CHUNK
    hpb = SCAN_HEADS_BWD

    def body(q_ref, k_ref, v_ref, g_ref, do_ref, st_ref, dq_ref, dk_ref, dv_ref, dg_ref, ds_scr, b_scr):
        consts = _scan_consts(rev)
        _, tri_t, lv, diag, diag_t = consts
        masks = [(lvl[0], lvl[1]) for lvl in lv] + [(diag, diag_t)]
        @pl.when(pl.program_id(1) == 0)
        def _():
            ds_scr[...] = jnp.zeros_like(ds_scr)

        tri = consts[0]
        dstate = {hh: ds_scr[hh] for hh in range(hpb)}

        def unit(hh, jj):
            sl = slice(hh * DH, (hh + 1) * DH)
            c = _chunk_in_block(SCAN_CB - 1 - jj, rev)
            rows = slice(c * CHUNK, (c + 1) * CHUNK)
            b_ref = b_scr.at[hh * SCAN_CB + jj]
            qc, kc, vc, gc = q_ref[rows, sl], k_ref[rows, sl], v_ref[rows, sl], g_ref[rows, sl]
            dob = do_ref[rows, sl].astype(BF16)
            vb = vc.astype(BF16)
            cum = _split_dot(tri, gc)
            b_ref[...] = cum
            da = _dot(dob, vb, NT)
            da_t = _dot(vb, dob, NT)
            yield
            terms = _chunk_terms(cum, b_ref, consts, rev)
            qf, kf = qc.astype(F32), kc.astype(F32)
            xs = [(jnp.where(tside, qf, kf) * w).astype(BF16) for w, (_, _, tside, _) in zip(terms[:-1], lv)]
            wqd, wkd = terms[-1]
            qdb, kdb = (qf * wqd).astype(BF16), (kf * wkd).astype(BF16)
            tot = _colsum(gc)
            e_tot = jnp.exp(tot)
            e_b = jnp.exp(cum)
            e_t = jnp.exp(tot - cum)
            qeb = (qf * e_b).astype(BF16)
            keb = (kf * e_t).astype(BF16)
            dsym = [(jnp.where(m, da, 0.0) + jnp.where(m_t, da_t, 0.0)).astype(BF16) for m, m_t in masks[:-1]]
            dad = (jnp.where(diag, da, 0.0).astype(BF16), jnp.where(diag_t, da_t, 0.0).astype(BF16))
            yield
            sym = [_dot(x, x, NT) for x in xs]
            dxs = [_dot(d, x, NN) for d, x in zip(dsym, xs)]
            at_d = _dot(kdb, qdb, NT)
            dqt_d = _dot(dad[0], kdb, NN)
            dkt_d = _dot(dad[1], qdb, NN)
            qd = _dot(dob, qeb, TN)
            yield
            a_t = jnp.where(diag_t, at_d, 0.0)
            dq = dqt_d * wqd
            dk = dkt_d * wkd
            db = dqt_d * qdb.astype(F32) - dkt_d * kdb.astype(F32)
            for s, dx, x, w, (_, m_t, tside, sgn) in zip(sym, dxs, xs, terms[:-1], lv):
                a_t = a_t + jnp.where(m_t, s, 0.0)
                dxw = dx * w
                dq = dq + jnp.where(tside, dxw, 0.0)
                dk = dk + jnp.where(tside, 0.0, dxw)
                db = db + (dx * x.astype(F32)) * sgn
            dv_intra = _dot(a_t.astype(BF16), dob, NN)
            st = st_ref[hh, c]
            stb = st.astype(BF16)
            dqe = _dot(dob, stb, NN)
            yield
            dst = dstate[hh]
            dstb = dst.astype(BF16)
            dstate[hh] = dst * e_tot + qd
            dv_ref[rows, sl] = (dv_intra + _dot(keb, dstb, NT)).astype(BF16)
            dke = _dot(vb, dstb, NN)
            yield
            qe = qeb.astype(F32)
            ke = keb.astype(F32)
            dq_ref[rows, sl] = (dq + dqe * e_b).astype(BF16)
            dk_ref[rows, sl] = (dk + dke * e_t).astype(BF16)
            db = db + dqe * qe - dke * ke
            dtot = _colsum(dstb.astype(F32) * stb.astype(F32)) * e_tot + _colsum(dke * ke)
            dg_ref[rows, sl] = _split_dot(tri_t, db) + dtot
            yield

        _run_staged([unit(hh, jj) for hh in range(hpb) for jj in range(SCAN_CB)])
        for hh in range(hpb):
            ds_scr[hh] = dstate[hh]

    ntb = t // SCAN_TB
    blk = lambda i: _block_order(ntb - 1 - i, ntb, rev)
    col = pl.BlockSpec((SCAN_TB, hpb * DH), lambda h, i: (blk(i), h))
    out = jax.ShapeDtypeStruct((t, D), F32)
    outb = jax.ShapeDtypeStruct((t, D), BF16)
    return pl.pallas_call(
        body, name="scan_bwd_" + ("bw" if rev else "fw"), grid=(NH // hpb, ntb),
        in_specs=[col] * 5 + [pl.BlockSpec((hpb, SCAN_CB, DH, DH), lambda h, i: (h, blk(i), 0, 0))],
        out_specs=[col] * 4,
        out_shape=[outb] * 3 + [out],
        scratch_shapes=[pltpu.VMEM((hpb, DH, DH), F32), pltpu.VMEM((hpb * SCAN_CB, CHUNK, DH), F32)],
        compiler_params=_cp("parallel", "arbitrary"),
    )(q, k, v, g, do, states)


W_IN_GRAD_CHUNKS = (("a", (0, 512)), ("b", (0, 128)), ("b", (128, 512)))
W_IN_REF = 6688


def _layout_w_in(w):
    return jnp.pad(w, ((0, 0), (0, W_IN_COLS - W_IN_REF)))


def _unlayout_w_in(d):
    return d[:, :W_IN_REF]


W_IN_PAD = 896


def _assemble_w_in(g):
    n, r, wp = g.shape
    tr = 256
    tiles = wp // DH

    def body(g_ref, o_ref):
        lane = lax.broadcasted_iota(jnp.int32, (tr, DH), 1)
        for t in range(W_IN_COLS // DH):
            acc = None
            for j in range(n):
                c = DH * t - W_IN_SHARD * j
                if c <= -DH or c >= W_IN_SHARD:
                    continue
                k, s = divmod(c, DH)
                lo = g_ref[j, :, k * DH:(k + 1) * DH] if 0 <= k < tiles else None
                hi = g_ref[j, :, (k + 1) * DH:(k + 2) * DH] if s and 0 <= k + 1 < tiles else None
                if s:
                    zero = jnp.zeros((tr, DH), g.dtype)
                    lo = zero if lo is None else pltpu.roll(lo, DH - s, 1)
                    hi = zero if hi is None else pltpu.roll(hi, DH - s, 1)
                    part = jnp.where(lane < DH - s, lo, hi)
                else:
                    part = lo
                acc = part if acc is None else acc + part
            o_ref[:, t * DH:(t + 1) * DH] = jnp.zeros((tr, DH), g.dtype) if acc is None else acc

    return pl.pallas_call(
        body, name="assemble_w_in", grid=(r // tr,),
        in_specs=[pl.BlockSpec((n, tr, wp), lambda i: (0, i, 0))],
        out_specs=pl.BlockSpec((tr, W_IN_COLS), lambda i: (i, 0)),
        out_shape=jax.ShapeDtypeStruct((r, W_IN_COLS), g.dtype),
        compiler_params=_cp("parallel"),
    )(g)


def _gate_cols(w):
    return jnp.pad(w, ((0, 0), (GOFF, GW - GOFF - D)))


def _gate_rows(w):
    return jnp.pad(w, ((GOFF, GW - GOFF - D), (0, 0)))


def _layout_wgk(w):
    r = w.shape[1]
    top = jnp.concatenate([w[0], jnp.zeros_like(w[0])], axis=1)
    bot = jnp.concatenate([jnp.zeros_like(w[1]), w[1]], axis=1)
    return jnp.concatenate([top, bot, jnp.zeros((DH - 2 * r, D), w.dtype)], axis=0)


def _unlayout_wgk(d, r=16):
    return jnp.stack([d[:r, :HW], d[r:2 * r, HW:]])


def _local_step(z, target, modc, modx, norms, onw, hg_lb, wgk, bgk, get_w_in, get_mix, get_ffn, send):
    n_pre1, n_post1, n_pre2, n_post2 = norms
    t = z[0].shape[0] + z[1].shape[0]
    tm = 1152 if t % 1152 == 0 else 256
    h1 = _prenorm(z, n_pre1, modc, modx, 0, 1, "prenorm1")
    w_in = get_w_in(h1)
    p = _matmul(h1, w_in, NN, BF16, "mm_in", t, 1024, D)
    q, v, k_f, k_b, g_f, g_b = _gates_fwd(p, hg_lb, wgk, bgk)
    o_f, st_f = _scan_fwd(q, k_f, v, g_f, False)
    o_b, st_b = _scan_fwd(q, k_b, v, g_b, True)
    y = _post_fwd(o_f, o_b, p, onw)
    w_br_hg, w_br_gla, w_out = get_mix(y)
    u1 = _matmul(y, w_br_hg, NN, BF16, "mm_br_hg", tm, GW, HW, a_off=0)
    u2 = _matmul(y, w_br_gla, NN, BF16, "mm_br_gla", tm, GW, HW, a_off=1)
    merged = _merge_fwd(p, u1, u2)
    y1 = _matmul(merged, w_out, NN, BF16, "mm_out", tm, 512, GW)
    z1, h2 = _mid_fwd(z, y1, n_post1, n_pre2, modc, modx)
    w_gu_t, w_down = get_ffn(h2)
    uv = _matmul(h2, w_gu_t, NT, BF16, "mm_gu", t, D_FF // 2, D)
    act = _swiglu_fwd(uv)
    y2 = _matmul(act, w_down, NN, BF16, "mm_down", t, 512, D_FF)
    dz, dy2, loss_vec, sm_final = _final(z1, y2, target, n_post2, modc, modx)
    dact = _matmul(dy2, w_down, NT, BF16, "mm_down_dx", t, D_FF // 2, D)
    d_w_down = _matmul(act, dy2, TN, BF16, "mm_down_dw", D_FF // 2, 1024, t)
    duv = _swiglu_bwd(uv, dact)
    dh2 = _matmul(duv, w_gu_t, NN, BF16, "mm_gu_dx", tm, 512, D_FF)
    d_w_gate_t = _matmul(duv, h2, TN, BF16, "mm_gate_dw", D_FF // 2, 1024, t, a_off=0, m_out=D_FF)
    d_w_up_t = _matmul(duv, h2, TN, BF16, "mm_up_dw", D_FF // 2, 1024, t, a_off=2, m_out=D_FF)
    dh2 = send(("w_down", "w_gate_t", "w_up_t"), (d_w_down, d_w_gate_t, d_w_up_t), dh2)
    dz, dy1, sm_mid = _mid_bwd(dh2, dz, z1, y1, n_post1, n_pre2, modc, modx)
    dmerged = _matmul(dy1, w_out, NT, BF16, "mm_out_dx", tm, GW, D)
    d_w_out = _matmul(merged, dy1, TN, BF16, "mm_out_dw", GW, 512, t)
    du1, du2, dgm = _merge_bwd(dmerged, p, u1, u2)
    dy_hg = _matmul(du1, w_br_hg, NT, BF16, "mm_br_hg_dx", tm, HW, GW)
    dy_gla = _matmul(du2, w_br_gla, NT, BF16, "mm_br_gla_dx", tm, HW, GW)
    d_w_br_hg = _matmul(y, du1, TN, BF16, "mm_br_hg_dw", HW, GW, t, a_off=0, m_out=HW)
    d_w_br_gla = _matmul(y, du2, TN, BF16, "mm_br_gla_dw", HW, GW, t, a_off=1, m_out=HW)
    dy_hg = send(("w_out", "w_br_hg", "w_br_gla"), (d_w_out, d_w_br_hg, d_w_br_gla), dy_hg)
    do, dgo, sm_post = _post_bwd(dy_hg, dy_gla, o_f, o_b, p, onw)
    dq_f, dk_f, dv_f, dg_f = _scan_bwd(q, k_f, v, g_f, do, st_f, False)
    dq_b, dk_b, dv_b, dg_b = _scan_bwd(q, k_b, v, g_b, do, st_b, True)
    dp, d_lb, d_wgk, d_bgk = _gates_bwd(p, hg_lb, wgk, bgk, dgm, dgo, dq_f, dq_b, dv_f, dv_b, dk_f, dk_b, dg_f, dg_b)
    d_w_in_a = _matmul(h1, dp, TN, BF16, "mm_in_dw_a", 512, 1024, t, a_off=0, m_out=D // 2)
    dp = send(("w_in_a",), (d_w_in_a,), dp)
    d_w_in_b = _matmul(h1, dp, TN, BF16, "mm_in_dw_b", 512, 1024, t, a_off=1, m_out=D // 2)
    dp = send(("w_in_b",), (d_w_in_b,), dp)
    dh1 = _matmul(dp, w_in, NT, BF16, "mm_in_dx", tm, 512, W_IN_COLS // 2)
    grad_x, sm_pre = _pre_bwd(dh1, dz, z, n_pre1, modc, modx)
    return dict(loss_vec=loss_vec, grad_x=grad_x, sm_final=sm_final, sm_mid=sm_mid, sm_post=sm_post, sm_pre=sm_pre,
                d_lb=d_lb, d_wgk=d_wgk, d_bgk=d_bgk)


MESH = pl.DeviceIdType.MESH
ANY = pl.BlockSpec(memory_space=pl.ANY)
N_REL = N_DEV - 1


def _place():
    return lax.axis_index("x"), lax.axis_index("y"), lax.axis_index("c")


def _slot(p):
    return 4 * p[0] + 2 * p[1] + p[2]


def _all_gather(arrays, name):
    n = len(arrays)

    def body(*refs):
        ins, outs = refs[:n], refs[n:2 * n]
        send_sems, recv_sems, local_sems = refs[2 * n:]
        x, y, c = _place()
        me, sibling = (x, y, c), (x, y, 1 - c)
        chips = [(1 - x, y), (x, 1 - y), (1 - x, 1 - y)]

        def copy(a, k, block, to, src=None):
            dst = outs[a].at[_slot(block)]
            return pltpu.make_async_remote_copy(
                src_ref=dst if src is None else src, dst_ref=dst,
                send_sem=send_sems.at[N_REL * a + k], recv_sem=recv_sems.at[N_REL * a + k],
                device_id=to, device_id_type=MESH)

        mine = [pltpu.make_async_copy(ins[a], outs[a].at[_slot(me)], local_sems.at[a]) for a in range(n)]
        for cp in mine:
            cp.start()
        first = []
        for a in range(n):
            first.append(copy(a, 0, me, sibling, src=ins[a]))
            first += [copy(a, 1 + j, me, (*chip, c), src=ins[a]) for j, chip in enumerate(chips)]
        for cp in first:
            cp.start()
        passed = []
        for j, chip in enumerate(chips):
            for a in range(n):
                copy(a, 1 + j, (*chip, c), me).wait_recv()
                fwd = copy(a, 4 + j, (*chip, c), sibling)
                fwd.start()
                passed.append(fwd)
        for a in range(n):
            copy(a, 0, sibling, me).wait_recv()
        for j, chip in enumerate(chips):
            for a in range(n):
                copy(a, 4 + j, (*chip, 1 - c), me).wait_recv()
        for cp in first + passed:
            cp.wait_send()
        for cp in mine:
            cp.wait()

    return pl.pallas_call(
        body, name=name,
        in_specs=[ANY] * n, out_specs=[ANY] * n,
        out_shape=[jax.ShapeDtypeStruct((N_DEV,) + a.shape, a.dtype) for a in arrays],
        scratch_shapes=[pltpu.SemaphoreType.DMA((N_REL * n,)), pltpu.SemaphoreType.DMA((N_REL * n,)),
                        pltpu.SemaphoreType.DMA((n,))],
    )(*arrays)


def _exchange(arrays, name):
    n = len(arrays)

    def body(*refs):
        ins, outs = refs[:n], refs[n:2 * n]
        send_sems, recv_sems, local_sems = refs[2 * n:]
        x, y, c = _place()
        me = _slot((x, y, c))
        mine = [pltpu.make_async_copy(ins[a].at[me], outs[a].at[me], local_sems.at[a]) for a in range(n)]
        for cp in mine:
            cp.start()
        copies = []
        for a in range(n):
            for k in range(1, N_DEV):
                flip = lambda v, bit: 1 - v if bit else v
                peer = (flip(x, k & 4), flip(y, k & 2), flip(c, k & 1))
                copies.append(pltpu.make_async_remote_copy(
                    src_ref=ins[a].at[_slot(peer)], dst_ref=outs[a].at[me],
                    send_sem=send_sems.at[N_REL * a + k - 1], recv_sem=recv_sems.at[N_REL * a + k - 1],
                    device_id=peer, device_id_type=MESH))
                copies[-1].start()
        i = 0
        for a in range(n):
            for k in range(1, N_DEV):
                flip = lambda v, bit: 1 - v if bit else v
                peer = (flip(x, k & 4), flip(y, k & 2), flip(c, k & 1))
                pltpu.make_async_remote_copy(
                    src_ref=ins[a].at[_slot(peer)], dst_ref=outs[a].at[_slot(peer)],
                    send_sem=send_sems.at[N_REL * a + k - 1], recv_sem=recv_sems.at[N_REL * a + k - 1],
                    device_id=peer, device_id_type=MESH).wait_recv()
                i += 1
        for cp in copies:
            cp.wait_send()
        for cp in mine:
            cp.wait()

    return pl.pallas_call(
        body, name=name,
        in_specs=[ANY] * n, out_specs=[ANY] * n,
        out_shape=[jax.ShapeDtypeStruct(a.shape, a.dtype) for a in arrays],
        scratch_shapes=[pltpu.SemaphoreType.DMA((N_REL * n,)), pltpu.SemaphoreType.DMA((N_REL * n,)),
                        pltpu.SemaphoreType.DMA((n,))],
    )(*arrays)


HBM = pl.BlockSpec(memory_space=pltpu.HBM)
SEM = pl.BlockSpec(memory_space=pltpu.SEMAPHORE)
EFFECT = pltpu.SideEffectType.DATAFLOW_SIDE_EFFECTING


def _peer_of(x, y, c, k):
    flip = lambda v, bit: 1 - v if bit else v
    return flip(x, k & 4), flip(y, k & 2), flip(c, k & 1)


def _view_whole(src, slot):
    return src


def _view_near(src, slot):
    return src


_view_near.peers = (1, 2, 4, 6)


def _view_block(src, slot):
    return src.at[slot]


W_IN_SHARD = W_IN_REF // N_DEV


def _view_window(rows):
    def view(src, slot):
        col0 = pl.multiple_of((W_IN_SHARD * slot // DH) * DH, DH)
        return src.at[pl.ds(rows[0], rows[1] - rows[0]), pl.ds(col0, D)]
    return view


def _split_copies(view, srcs, lands, send_sems, recv_sems, local_sems):
    x, y, c = _place()
    me = _slot((x, y, c))
    local, sends, waits = [], [], []
    for a, (src, land) in enumerate(zip(srcs, lands)):
        local.append(pltpu.make_async_copy(view(src, me), land.at[me], local_sems.at[a]))
        for k in getattr(view, "peers", range(1, N_DEV)):
            peer = _peer_of(x, y, c, k)
            mine = view(src, _slot(peer))
            sems = dict(send_sem=send_sems.at[N_REL * a + k - 1], recv_sem=recv_sems.at[N_REL * a + k - 1],
                        device_id=peer, device_id_type=MESH)
            sends.append(pltpu.make_async_remote_copy(src_ref=mine, dst_ref=land.at[me], **sems))
            waits.append(pltpu.make_async_remote_copy(src_ref=mine, dst_ref=land.at[_slot(peer)], **sems))
    return local, sends, waits


def _split_start(view, land_shapes, srcs, name, after):
    n = len(srcs)
    lands = [lax.empty(shp, s.dtype) for shp, s in zip(land_shapes, srcs)]

    def body(*refs):
        src_refs, land_refs = refs[:n], refs[n:2 * n]
        send_sems, recv_sems, local_sems = refs[2 * n + 1:2 * n + 4]
        token = refs[-1]
        local, sends, _ = _split_copies(view, src_refs, land_refs, send_sems, recv_sems, local_sems)
        for cp in local + sends:
            cp.start()
        token[...] = jnp.zeros_like(token)

    hbm = lambda a: pltpu.with_memory_space_constraint(a, pltpu.HBM)
    out = pl.pallas_call(
        body, name=name,
        out_shape=(pltpu.SemaphoreType.DMA((N_REL * n,)), pltpu.SemaphoreType.DMA((N_REL * n,)),
                   pltpu.SemaphoreType.DMA((n,)),
                   *[pltpu.HBM(s.shape, s.dtype) for s in srcs], *[pltpu.HBM(l.shape, l.dtype) for l in lands],
                   jax.ShapeDtypeStruct((8, DH), F32)),
        in_specs=[HBM] * (2 * n) + [ANY],
        out_specs=(SEM, SEM, SEM, *([HBM] * (2 * n)), pl.BlockSpec(memory_space=pltpu.VMEM)),
        input_output_aliases={i: 3 + i for i in range(2 * n)},
        compiler_params=pltpu.CompilerParams(has_side_effects=EFFECT),
    )(*[hbm(s) for s in srcs], *[hbm(l) for l in lands], after)
    handle = dict(view=view, n=n, sems=out[:3], srcs=list(out[3:3 + n]), lands=list(out[3 + n:3 + 2 * n]))
    return handle, out[-1]


def _split_wait(handle, name, after, srcs=None):
    view, n, sems, lands = handle["view"], handle["n"], handle["sems"], handle["lands"]
    srcs = handle["srcs"] if srcs is None else srcs
    afters = list(after) if isinstance(after, (list, tuple)) else [after]

    def body(*refs):
        src_refs, land_refs = refs[:n], refs[n:2 * n]
        send_sems, recv_sems, local_sems = refs[2 * n:2 * n + 3]
        local, _, waits = _split_copies(view, src_refs, land_refs, send_sems, recv_sems, local_sems)
        for cp in waits:
            cp.wait_send()
            cp.wait_recv()
        for cp in local:
            cp.wait()

    out = pl.pallas_call(
        body, name=name,
        out_shape=(*[pltpu.HBM(s.shape, s.dtype) for s in srcs], *[pltpu.HBM(l.shape, l.dtype) for l in lands]),
        in_specs=[HBM] * (2 * n) + [SEM, SEM, SEM] + [ANY] * len(afters),
        out_specs=tuple([HBM] * (2 * n)),
        input_output_aliases={i: i for i in range(2 * n)},
        compiler_params=pltpu.CompilerParams(has_side_effects=EFFECT),
    )(*srcs, *lands, *sems, *afters)
    handle["srcs"] = list(out[:n])
    return list(out[n:])


def _tie(x, token, name):
    def body(x_ref, t_ref, o_ref):
        pass

    return pl.pallas_call(
        body, name=name, out_shape=jax.ShapeDtypeStruct(x.shape, x.dtype),
        in_specs=[ANY, ANY], out_specs=ANY, input_output_aliases={0: 0},
    )(x, token)


def _forward_to_sibling(land, name):
    def body(land_ref, out_ref, send_sems, recv_sems):
        x, y, c = _place()
        sibling = (x, y, 1 - c)
        chips = [(1 - x, y), (x, 1 - y), (1 - x, 1 - y)]

        def copy(j, core):
            blk = _slot((*chips[j], core))
            return pltpu.make_async_remote_copy(src_ref=land_ref.at[blk], dst_ref=out_ref.at[blk],
                                                send_sem=send_sems.at[j], recv_sem=recv_sems.at[j],
                                                device_id=sibling, device_id_type=MESH)

        sends = [copy(j, c) for j in range(3)]
        for cp in sends:
            cp.start()
        for j in range(3):
            copy(j, 1 - c).wait_recv()
        for cp in sends:
            cp.wait_send()

    return pl.pallas_call(
        body, name=name, in_specs=[ANY], out_specs=ANY, input_output_aliases={0: 0},
        out_shape=jax.ShapeDtypeStruct(land.shape, land.dtype),
        scratch_shapes=[pltpu.SemaphoreType.DMA((3,)), pltpu.SemaphoreType.DMA((3,))],
    )(land)


def _mod_fwd(a, w, b):
    def body(a_ref, w_ref, b_ref, o_ref):
        o_ref[...] = _dot(_silu(a_ref[...]), w_ref[...], NN, precision=HI) + b_ref[...]

    return pl.pallas_call(
        body, name="mod_fwd", out_shape=jax.ShapeDtypeStruct((a.shape[0], w.shape[1]), F32),
        compiler_params=pltpu.CompilerParams(vmem_limit_bytes=VMEM_LIMIT),
    )(a, w, b)


def _mod_bwd(a, d, w):
    def body(a_ref, d_ref, w_ref, dw_ref, dc_ref):
        av = a_ref[...]
        dv = d_ref[...]
        dw_ref[...] = _dot(_silu(av), dv, TN, precision=HI)
        da = _dot(dv[0:8, :], w_ref[...], NT, precision=HI) * _dsilu(av[0:8, :])
        row = lax.broadcasted_iota(jnp.int32, da.shape, 0)
        dc_ref[...] = jnp.where(row == 0, da, 0.0)

    return pl.pallas_call(
        body, name="mod_bwd",
        out_shape=[jax.ShapeDtypeStruct(w.shape, F32), jax.ShapeDtypeStruct((8, w.shape[0]), F32)],
        compiler_params=pltpu.CompilerParams(vmem_limit_bytes=VMEM_LIMIT),
    )(a, d, w)


def _sum_devices(g):
    def body(g_ref, o_ref):
        acc = g_ref[0]
        for i in range(1, g.shape[0]):
            acc = acc + g_ref[i]
        o_ref[...] = acc

    return pl.pallas_call(body, name="sum_devices_%d" % g.shape[1],
                          out_shape=jax.ShapeDtypeStruct(g.shape[1:], F32))(g)


def _sum_windows(g, name):
    n, r, c = g.shape
    tr = 128

    def body(g_ref, o_ref):
        x, y, cc = _place()
        lane0 = (W_IN_SHARD * _slot((x, y, cc))) % DH
        acc = g_ref[0].astype(F32)
        for i in range(1, n):
            acc = acc + g_ref[i].astype(F32)
        o_ref[...] = pltpu.roll(acc, (c - lane0) % c, 1).T

    return pl.pallas_call(
        body, name=name, grid=(r // tr,),
        in_specs=[pl.BlockSpec((n, tr, c), lambda i: (0, i, 0))],
        out_specs=pl.BlockSpec((c, tr), lambda i: (0, i)),
        out_shape=jax.ShapeDtypeStruct((c, r), F32),
        compiler_params=_cp("parallel"),
    )(g)


def _adam_rows(r, c, n):
    budget = 6 * 1024 * 1024
    best = None
    for tr in range(16, r + 1, 16):
        if r % tr == 0 and tr * c * (2 * n + 28) <= budget:
            best = tr
    return best if best is not None else r


def _adamw(g, w, m, v, name):
    n, r, c = g.shape
    tr = _adam_rows(r, c, n)
    bc1 = 1.0 - ADAM_B1 ** ADAM_STEP
    bc2 = 1.0 - ADAM_B2 ** ADAM_STEP

    def body(g_ref, w_ref, m_ref, v_ref, go_ref, d_ref, mo_ref, vo_ref):
        grad = g_ref[0].astype(F32)
        for i in range(1, n):
            grad = grad + g_ref[i].astype(F32)
        go_ref[...] = grad
        m_new = ADAM_B1 * m_ref[...] + (1.0 - ADAM_B1) * grad
        v_new = ADAM_B2 * v_ref[...] + (1.0 - ADAM_B2) * (grad * grad)
        mo_ref[...] = m_new
        vo_ref[...] = v_new
        d_ref[...] = -ADAM_LR * ((m_new / bc1) / (jnp.sqrt(v_new / bc2) + ADAM_EPS) + ADAM_WD * w_ref[...])

    blk = pl.BlockSpec((tr, c), lambda i: (i, 0))
    out = jax.ShapeDtypeStruct((r, c), F32)
    return pl.pallas_call(
        body, name=name, grid=(r // tr,),
        in_specs=[pl.BlockSpec((n, tr, c), lambda i: (0, i, 0)), blk, blk, blk],
        out_specs=[blk] * 4, out_shape=[out] * 4,
        compiler_params=_cp("parallel"),
    )(g, w, m, v)


def kernel(x, c, ctx, c_ctx, w_mod, b_mod, norm_pre1, norm_post1, norm_pre2, norm_post2, w_in, hg_lb, hg_onorm, gla_w_gk, gla_b_gk, gla_onorm, w_br_hg, w_br_gla, w_out, w_ff_gate, w_ff_up, w_ff_down, loss_target, m_c_ctx, m_w_mod, m_b_mod, m_norm_pre1, m_norm_post1, m_norm_pre2, m_norm_post2, m_w_in, m_hg_lb, m_hg_onorm, m_gla_w_gk, m_gla_b_gk, m_gla_onorm, m_w_br_hg, m_w_br_gla, m_w_out, m_w_ff_gate, m_w_ff_up, m_w_ff_down, v_c_ctx, v_w_mod, v_b_mod, v_norm_pre1, v_norm_post1, v_norm_pre2, v_norm_post2, v_w_in, v_hg_lb, v_hg_onorm, v_gla_w_gk, v_gla_b_gk, v_gla_onorm, v_w_br_hg, v_w_br_gla, v_w_out, v_w_ff_gate, v_w_ff_up, v_w_ff_down):
    xi, yi, ci = lax.axis_index("x"), lax.axis_index("y"), lax.axis_index("c")
    me = 4 * xi + 2 * yi + ci
    t = CTX + x.shape[1]

    c_all, lb_g, wgk_g, bgk_g = _all_gather([c, hg_lb, gla_w_gk[0], gla_b_gk[0]], "ag_small")
    tr_ = lambda a: jnp.swapaxes(a[0], 0, 1)
    big = [w_in[0], w_br_hg[0], w_br_gla[0], w_out[0], tr_(w_ff_gate), tr_(w_ff_up), w_ff_down[0]]
    big_bf = [w.astype(BF16) for w in big]
    big_bf[0] = jnp.pad(big_bf[0], ((0, 0), (0, W_IN_PAD - W_IN_SHARD)))
    cols = lambda g: jnp.transpose(g, (1, 0, 2)).reshape(g.shape[1], N_DEV * g.shape[2])

    def get_w_in(after):
        land, = _split_wait(w_in_handle, "ag_w_in_wait", after)
        return _assemble_w_in(_forward_to_sibling(land, "ag_w_in_forward"))

    def get_mix(after):
        g_brh, g_brg, g_out = _split_wait(mix_handle, "ag_mix_wait", after)
        return _gate_cols(cols(g_brh)), _gate_cols(cols(g_brg)), _gate_rows(g_out.reshape(D, D))

    def get_ffn(after):
        g_gate, g_up, g_down = _split_wait(ffn_handle, "ag_ffn_wait", after)
        return (g_gate.reshape(D_FF, D), g_up.reshape(D_FF, D)), g_down.reshape(D_FF, D)

    hg_lb_full = jnp.transpose(lb_g, (1, 2, 0, 3)).reshape(2, 2, HW)
    wgk_k = _layout_wgk(jnp.transpose(wgk_g, (1, 2, 0, 3)).reshape(2, 16, HW)).astype(BF16)
    bgk_k = jnp.transpose(bgk_g, (1, 0, 2)).reshape(1, D)
    onw = jnp.concatenate([jnp.tile(hg_onorm, (1, NH // 2)), jnp.tile(gla_onorm, (1, NH // 2))], axis=1)

    n_mod = w_mod.shape[2]
    a9 = jnp.concatenate([c_ctx[None], c_all[:, 0], jnp.zeros((16 - 1 - N_DEV, D), F32)], axis=0)
    b_loc = lax.dynamic_slice(b_mod, (0, me * n_mod), (1, n_mod))
    s_loc = _mod_fwd(a9, w_mod[0], b_loc)
    s_all, = _all_gather([s_loc], "ag_mod")
    mod_all = jnp.transpose(s_all, (1, 0, 2)).reshape(16, N_DEV * n_mod)
    pad8 = lambda m: jnp.concatenate([m.reshape(6, D), jnp.zeros((2, D), F32)], axis=0)
    modc = pad8(mod_all[0])
    modx = pad8(lax.dynamic_slice(mod_all, (1 + me, 0), (1, N_DEV * n_mod))[0])

    gathered = lambda arrs: [(N_DEV,) + a.shape for a in arrs]
    w_in_handle, tok = _split_start(_view_near, gathered(big_bf[:1]), big_bf[:1], "ag_w_in_start", s_all)
    mix_handle, tok = _split_start(_view_whole, gathered(big_bf[1:4]), big_bf[1:4], "ag_mix_start", tok)
    ffn_handle, tok = _split_start(_view_whole, gathered(big_bf[4:]), big_bf[4:], "ag_ffn_start", tok)

    z = (ctx[0], x[0])
    modx = _tie(modx, tok, "tie_mod")
    norms = (norm_pre1, norm_post1, norm_pre2, norm_post2)
    shard = lambda d: jnp.transpose(d.reshape(d.shape[0], N_DEV, -1), (1, 0, 2)).astype(BF16)
    rowshard = lambda d: d.reshape(N_DEV, d.shape[0] // N_DEV, d.shape[1]).astype(BF16)
    sent, w_in_grad = [], {}

    def send_w_in(i, x_after):
        half, rows = W_IN_GRAD_CHUNKS[i]
        handle, tok = _split_start(_view_window(rows), [(N_DEV, rows[1] - rows[0], D)], w_in_grad[half],
                                   "grads_w_in%d_start" % i, x_after)
        w_in_grad[half] = handle["srcs"]
        sent.append(("w_in%d" % i, ["w_in#%d" % i], handle))
        return _tie(x_after, tok, "tie_w_in%d" % i)

    def send(names, grads, x_after):
        if names == ("w_in_a",):
            w_in_grad["a"] = list(grads)
            return send_w_in(0, x_after)
        if names == ("w_in_b",):
            w_in_grad["b"] = list(grads)
            return x_after
        arrs, leaves = [], []
        for nm, g in zip(names, grads):
            if nm in ("w_gate_t", "w_up_t"):
                arrs.append(rowshard(g))
                leaves.append({"w_gate_t": "w_ff_gate", "w_up_t": "w_ff_up"}[nm])
            elif nm == "w_down":
                arrs.append(rowshard(g))
                leaves.append("w_ff_down")
            elif nm == "w_out":
                arrs.append(rowshard(g[GOFF:GOFF + D]))
                leaves.append(nm)
            else:
                arrs.append(shard(g[:, GOFF:GOFF + D]))
                leaves.append(nm)
        handle, tok = _split_start(_view_block, [a.shape for a in arrs], arrs, "grads_%s_start" % names[0], x_after)
        sent.append((names[0], leaves, handle))
        return _tie(x_after, tok, "tie_" + names[0])

    r = _local_step(z, loss_target[0], modc, modx, norms, onw, hg_lb_full, wgk_k, bgk_k,
                    get_w_in, get_mix, get_ffn, send)
    grad_x = r["grad_x"][None]

    sm_pre, sm_mid, sm_fin = r["sm_pre"], r["sm_mid"], r["sm_final"]
    dmodc = jnp.stack([sm_pre[0], sm_pre[2], sm_mid[4], sm_mid[0], sm_mid[2], sm_fin[0]]).reshape(-1)
    dmodx = jnp.stack([sm_pre[1], sm_pre[3], sm_mid[5], sm_mid[1], sm_mid[3], sm_fin[1]]).reshape(-1)
    on = r["sm_post"][0].reshape(NH, DH)
    pieces = [dmodc, dmodx, sm_pre[4], sm_mid[7], sm_mid[6], sm_fin[2], on[:NH // 2].sum(0), on[NH // 2:].sum(0),
              r["d_lb"][:2].reshape(-1), _unlayout_wgk(r["d_wgk"]).reshape(-1), r["d_bgk"][0]]
    loss_local = (0.5 / D) * jnp.sum(r["loss_vec"])
    pieces.append(jnp.concatenate([loss_local.reshape(1), jnp.zeros((DH - 1,), F32)]))
    sizes = [p.shape[0] for p in pieces]
    pack = jnp.concatenate(pieces).reshape(-1, DH)
    pack_all, = _all_gather([pack], "ag_small_grads")
    pack_all = send_w_in(1, pack_all)
    tot = _sum_devices(pack_all).reshape(-1)
    offs = [sum(sizes[:i]) for i in range(len(sizes))]
    part = lambda i: tot[offs[i]:offs[i] + sizes[i]]
    dmodc_t, dmodx_t = part(0), part(1)
    g_b_mod = (dmodc_t + dmodx_t)[None]
    g_norms = [part(i)[None] for i in (2, 3, 4, 5)]
    g_hg_on, g_gla_on = part(6)[None], part(7)[None]
    lb0 = lax.dynamic_slice(part(8).reshape(2, HW), (0, me * (HW // N_DEV)), (2, HW // N_DEV))
    g_hg_lb = jnp.stack([lb0, -lb0])
    g_wgk = lax.dynamic_slice(part(9).reshape(2, 16, HW), (0, 0, me * (HW // N_DEV)), (2, 16, HW // N_DEV))[None]
    g_bgk = lax.dynamic_slice(part(10).reshape(2, HW), (0, me * (HW // N_DEV)), (2, HW // N_DEV))[None]
    loss = part(11)[0]

    dmx_all = pack_all.reshape(N_DEV, -1)[:, sizes[0]:sizes[0] + sizes[1]]
    d9 = jnp.concatenate([lax.dynamic_slice(dmodc_t[None], (0, me * n_mod), (1, n_mod)),
                          lax.dynamic_slice(dmx_all, (0, me * n_mod), (N_DEV, n_mod)),
                          jnp.zeros((16 - 1 - N_DEV, n_mod), F32)], axis=0)
    g_w_mod, dcc_part = _mod_bwd(a9, d9, w_mod[0])
    dcc_all, = _all_gather([dcc_part], "ag_c_ctx")
    dcc_all = send_w_in(2, dcc_all)
    g_c_ctx = _sum_devices(dcc_all)[0]

    recv = {}
    for first, leaves, handle in sent:
        if not first.startswith("w_in"):
            recv.update(zip(leaves, _split_wait(handle, "grads_%s_wait" % first, g_c_ctx)))
    moms = [(m_w_in, v_w_in), (m_w_br_hg, v_w_br_hg), (m_w_br_gla, v_w_br_gla), (m_w_out, v_w_out),
            (m_w_ff_gate, v_w_ff_gate), (m_w_ff_up, v_w_ff_up), (m_w_ff_down, v_w_ff_down)]
    names = ["w_in", "w_br_hg", "w_br_gla", "w_out", "w_ff_gate", "w_ff_up", "w_ff_down"]
    res = {}

    def update(nm, w, m, v):
        if nm in ("w_ff_gate", "w_ff_up"):
            outs = _adamw(recv[nm], w, tr_(m), tr_(v), "adamw_" + nm)
            res[nm] = [jnp.swapaxes(o, 0, 1)[None] for o in outs]
        else:
            res[nm] = [o[None] for o in _adamw(recv[nm], w, m[0], v[0], "adamw_" + nm)]

    for nm, w, (m, v) in list(zip(names, big, moms))[1:]:
        update(nm, w, m, v)
    res["w_mod"] = [o[None] for o in _adamw(g_w_mod[None], w_mod[0], m_w_mod[0], v_w_mod[0], "adamw_w_mod")]

    small = [("c_ctx", c_ctx, m_c_ctx, v_c_ctx, g_c_ctx), ("b_mod", b_mod, m_b_mod, v_b_mod, g_b_mod),
             ("norm_pre1", norm_pre1, m_norm_pre1, v_norm_pre1, g_norms[0]),
             ("norm_post1", norm_post1, m_norm_post1, v_norm_post1, g_norms[1]),
             ("norm_pre2", norm_pre2, m_norm_pre2, v_norm_pre2, g_norms[2]),
             ("norm_post2", norm_post2, m_norm_post2, v_norm_post2, g_norms[3]),
             ("hg_lb", hg_lb, m_hg_lb, v_hg_lb, g_hg_lb), ("hg_onorm", hg_onorm, m_hg_onorm, v_hg_onorm, g_hg_on),
             ("gla_w_gk", gla_w_gk, m_gla_w_gk, v_gla_w_gk, g_wgk), ("gla_b_gk", gla_b_gk, m_gla_b_gk, v_gla_b_gk, g_bgk),
             ("gla_onorm", gla_onorm, m_gla_onorm, v_gla_onorm, g_gla_on)]
    flat = lambda k: jnp.concatenate([s[k].reshape(-1) for s in small]).reshape(-1, DH)
    outs = _adamw(flat(4)[None], flat(1), flat(2), flat(3), "adamw_small")
    off = 0
    for nm, w, _, _, _ in small:
        res[nm] = [o.reshape(-1)[off:off + w.size].reshape(w.shape) for o in outs]
        off += w.size

    done = [res[nm][0] for nm in names[1:]] + [res["w_mod"][0], outs[0]]
    sums = []
    for i, (first, leaves, handle) in enumerate(s for s in sent if s[0].startswith("w_in")):
        half = W_IN_GRAD_CHUNKS[i][0]
        land, = _split_wait(handle, "grads_%s_wait" % first, done, srcs=w_in_grad[half])
        w_in_grad[half] = handle["srcs"]
        sums.append(_sum_windows(land, "sum_windows%d" % i))
    g_t = jnp.concatenate(sums, axis=1)[:W_IN_SHARD]
    lin = lambda a: a.reshape(W_IN_SHARD * D // DH, DH)
    major = lambda a: lin(jnp.transpose(a, (2, 0, 1)))
    outs = _adamw(lin(g_t)[None], major(w_in), major(m_w_in), major(v_w_in), "adamw_w_in")
    res["w_in"] = [jnp.transpose(o.reshape(W_IN_SHARD, 1, D), (1, 2, 0)) for o in outs]

    order = ["c_ctx", "w_mod", "b_mod", "norm_pre1", "norm_post1", "norm_pre2", "norm_post2", "w_in", "hg_lb",
             "hg_onorm", "gla_w_gk", "gla_b_gk", "gla_onorm", "w_br_hg", "w_br_gla", "w_out", "w_ff_gate", "w_ff_up",
             "w_ff_down"]
    return (loss, grad_x, *[res[n][k] for k in range(4) for n in order])
```

```python
import functools

import jax
import jax.numpy as jnp
from jax import lax
from jax.experimental import pallas as pl
from jax.experimental.pallas import tpu as pltpu

F32 = jnp.float32
BF16 = jnp.bfloat16
HI = lax.Precision.HIGHEST

N_DEV = 8
D = 1024
CTX = 256
HW = 512
DH = 128
NH = 8
D_FF = 2816
EPS = 1e-6
GLA_NORM = 16.0
CHUNK = 64
TR = 256
NCT = CTX // TR
W_IN_COLS = 7168
MAIN0 = 0
LR0 = 4608
GW = 1152
GOFF = 32
GATE_HG0 = LR0
GATE_GLA0 = LR0 + D
LEVELS = (32, 16, 8)
EXP_CLAMP = 80.0
VMEM_LIMIT = 48 * 1024 * 1024

ADAM_LR, ADAM_B1, ADAM_B2, ADAM_EPS, ADAM_WD, ADAM_STEP = 0.001, 0.9, 0.999, 1e-08, 0.01, 10


def _cp(*sem):
    return pltpu.CompilerParams(dimension_semantics=sem, vmem_limit_bytes=VMEM_LIMIT)


def _sig(x):
    return jax.nn.sigmoid(x)


def _silu(x):
    return x * _sig(x)


def _dsilu(x):
    s = _sig(x)
    return s * (1.0 + x * (1.0 - s))


def _rstd(x):
    return lax.rsqrt(jnp.mean(x * x, axis=-1, keepdims=True) + EPS)


def _rms_bwd(a, y, r):
    return r * (a - y * (r * r) * jnp.mean(a * y, axis=-1, keepdims=True))


def _colsum(x):
    return jnp.sum(x, axis=0, keepdims=True)


def _dot(a, b, dims, precision=None):
    return lax.dot_general(a, b, (dims, ((), ())), preferred_element_type=F32, precision=precision)


NN = ((1,), (0,))
NT = ((1,), (1,))
TN = ((0,), (0,))

SCAN_HEADS_FWD = 4
SCAN_HEADS_BWD = 4


def _split_dot(m, x):
    mb = m.astype(BF16)
    x1 = x.astype(BF16)
    r1 = x - x1.astype(F32)
    x2 = r1.astype(BF16)
    x3 = (r1 - x2.astype(F32)).astype(BF16)
    return _dot(mb, x1, NN) + _dot(mb, x2, NN) + _dot(mb, x3, NN)


def _matmul(a, b, dims, out_dtype, name, tm, tn, tk, a_off=0, m_out=None):
    pair = isinstance(b, (tuple, list))
    bs = list(b) if pair else [b]
    b1 = bs[0]
    rows = b1.shape[0] * len(bs)
    half = None
    if dims == NN:
        m, k, n = a.shape[0], rows, b1.shape[1]
        a_spec = pl.BlockSpec((tm, tk), lambda i, j, kk: (i, kk + a_off))
        half = b1.shape[0] // tk
        b_maps = [lambda i, j, kk: (kk, j)] if not pair else [
            lambda i, j, kk: (jnp.minimum(kk, half - 1), j), lambda i, j, kk: (jnp.maximum(kk - half, 0), j)]
        b_specs = [pl.BlockSpec((tk, tn), f) for f in b_maps]
        axis = 2
    elif dims == NT:
        m, k, n = a.shape[0], b1.shape[1], rows
        a_spec = pl.BlockSpec((tm, tk), lambda i, j, kk: (i, kk + a_off))
        half = b1.shape[0] // tn
        b_maps = [lambda i, j, kk: (j, kk)] if not pair else [
            lambda i, j, kk: (jnp.minimum(j, half - 1), kk), lambda i, j, kk: (jnp.maximum(j - half, 0), kk)]
        b_specs = [pl.BlockSpec((tn, tk), f) for f in b_maps]
        axis = 1
    else:
        assert not pair
        m, k = (a.shape[1] if m_out is None else m_out), a.shape[0]
        n = b1.shape[1]
        a_spec = pl.BlockSpec((tk, tm), lambda i, j, kk: (kk, i + a_off))
        b_specs = [pl.BlockSpec((tk, tn), lambda i, j, kk: (kk, j))]
    assert m % tm == 0 and n % tn == 0 and k % tk == 0, (name, m, n, k, tm, tn, tk)
    nk = k // tk
    nb = len(bs)

    def body(a_ref, *refs):
        o_ref = refs[nb]
        if pair:
            bv = jnp.where(pl.program_id(axis) < half, refs[0][...], refs[1][...])
        else:
            bv = refs[0][...]
        part = _dot(a_ref[...], bv, dims)
        if nk == 1:
            o_ref[...] = part.astype(o_ref.dtype)
            return
        acc_ref = refs[nb + 1]
        kk = pl.program_id(2)

        @pl.when(kk == 0)
        def _():
            acc_ref[...] = part

        @pl.when(kk > 0)
        def _():
            acc_ref[...] += part

        @pl.when(kk == nk - 1)
        def _():
            o_ref[...] = acc_ref[...].astype(o_ref.dtype)

    return pl.pallas_call(
        body,
        name=name,
        grid=(m // tm, n // tn, nk),
        in_specs=[a_spec] + b_specs,
        out_specs=pl.BlockSpec((tm, tn), lambda i, j, kk: (i, j)),
        out_shape=jax.ShapeDtypeStruct((m, n), out_dtype),
        scratch_shapes=[] if nk == 1 else [pltpu.VMEM((tm, tn), F32)],
        compiler_params=_cp("parallel", "parallel", "arbitrary"),
    )(a, *bs)


def _row(c):
    return pl.BlockSpec((TR, c), lambda i: (i, 0))


def _rowcol(width, cb):
    return pl.BlockSpec((TR, width), lambda i: (i, cb))


def _full(shape):
    return pl.BlockSpec(shape, lambda i: (0,) * len(shape))


def _mod_row(mc_ref, mx_ref, k, is_ctx):
    return jnp.where(is_ctx, mc_ref[k:k + 1, :], mx_ref[k:k + 1, :])


def _z_specs():
    return [pl.BlockSpec((TR, D), lambda i: (jnp.minimum(i, NCT - 1), 0)),
            pl.BlockSpec((TR, D), lambda i: (jnp.maximum(i - NCT, 0), 0))]


def _z_tile(c_ref, x_ref, is_ctx):
    return jnp.where(is_ctx, c_ref[...], x_ref[...])


def _acc_row(ref, k, val):
    ref[k:k + 1, :] += val


def _acc_mod(ref, k, is_ctx, val):
    zero = jnp.zeros_like(val)
    ref[k:k + 1, :] += jnp.where(is_ctx, val, zero)
    ref[k + 1:k + 2, :] += jnp.where(is_ctx, zero, val)


def _prenorm(z, nw, modc, modx, i_shift, i_scale, name):
    t = z[0].shape[0] + z[1].shape[0]

    def body(zc_ref, zx_ref, nw_ref, mc_ref, mx_ref, h_ref):
        is_ctx = pl.program_id(0) < NCT
        x = _z_tile(zc_ref, zx_ref, is_ctx)
        n = x * _rstd(x) * nw_ref[...]
        h = n * (1.0 + _mod_row(mc_ref, mx_ref, i_scale, is_ctx)) + _mod_row(mc_ref, mx_ref, i_shift, is_ctx)
        h_ref[...] = h.astype(BF16)

    return pl.pallas_call(
        body, name=name, grid=(t // TR,),
        in_specs=_z_specs() + [_full((1, D)), _full((8, D)), _full((8, D))],
        out_specs=_row(D),
        out_shape=jax.ShapeDtypeStruct((t, D), BF16),
        compiler_params=_cp("parallel"),
    )(*z, nw, modc, modx)


def _hg_lb(lb_ref, d):
    a0 = lb_ref[0, d:d + 1, :]
    a1 = lb_ref[1, d:d + 1, :]
    mx = jnp.maximum(a0, a1)
    e0 = jnp.exp(a0 - mx)
    e1 = jnp.exp(a1 - mx)
    return e0 / (e0 + e1)


def _log_sigmoid(x):
    return jnp.minimum(x, 0.0) - jnp.log(1.0 + jnp.exp(-jnp.abs(x)))


def _gates_fwd(p, hg_lb, wgk, bgk):
    t = p.shape[0]
    seg = lambda j: _rowcol(HW, MAIN0 // HW + j)

    def body(hq_ref, hi_ref, hf_ref, hb_ref, gq_ref, gk_ref, gv_ref, lr_ref, lb_ref, wgk_ref, bgk_ref,
             q_ref, v_ref, kf_ref, kb_ref, gf_ref, gb_ref):
        q_ref[:, :HW] = _silu(hq_ref[...].astype(F32)).astype(BF16)
        q_ref[:, HW:] = (gq_ref[...].astype(F32) * (DH ** -0.5)).astype(BF16)
        v_ref[:, :HW] = hi_ref[...]
        v_ref[:, HW:] = gv_ref[...]
        xg = _dot(lr_ref[...].astype(BF16), wgk_ref[...], NN) + bgk_ref[...]
        for d, (raw_ref, k_ref, g_ref) in enumerate(((hf_ref, kf_ref, gf_ref), (hb_ref, kb_ref, gb_ref))):
            lbd = _hg_lb(lb_ref, d)
            f = lbd + (1.0 - lbd) * _sig(raw_ref[...].astype(F32))
            k_ref[:, :HW] = (1.0 - f).astype(BF16)
            k_ref[:, HW:] = gk_ref[...]
            g_ref[:, :HW] = jnp.log(f)
            g_ref[:, HW:] = _log_sigmoid(xg[:, d * HW:(d + 1) * HW]) * (1.0 / GLA_NORM)

    out = jax.ShapeDtypeStruct((t, D), F32)
    outb = jax.ShapeDtypeStruct((t, D), BF16)
    return pl.pallas_call(
        body, name="gates_fwd", grid=(t // TR,),
        in_specs=[seg(0), seg(1), seg(2), seg(3), seg(5), seg(6), seg(7), _rowcol(DH, LR0 // DH),
                  _full((2, 2, HW)), _full((DH, D)), _full((1, D))],
        out_specs=[_row(D)] * 6,
        out_shape=[outb] * 4 + [out] * 2,
        compiler_params=_cp("parallel"),
    )(p, p, p, p, p, p, p, p, hg_lb, wgk, bgk)


def _post_fwd(o_fw, o_bw, p, onw):
    t = o_fw.shape[0]

    def body(of_ref, ob_ref, g1_ref, g2_ref, w_ref, y_ref):
        for h in range(NH):
            sl = slice(h * DH, (h + 1) * DH)
            o = of_ref[:, sl] + ob_ref[:, sl]
            g_ref = g1_ref if h < NH // 2 else g2_ref
            gs = slice((h % (NH // 2)) * DH, (h % (NH // 2) + 1) * DH)
            n = o * _rstd(o) * w_ref[:, sl]
            y_ref[:, sl] = (n * _silu(g_ref[:, gs].astype(F32))).astype(BF16)

    return pl.pallas_call(
        body, name="post_fwd", grid=(t // TR,),
        in_specs=[_row(D), _row(D), _rowcol(HW, MAIN0 // HW + 4), _rowcol(HW, MAIN0 // HW + 8), _full((1, D))],
        out_specs=_row(D),
        out_shape=jax.ShapeDtypeStruct((t, D), BF16),
        compiler_params=_cp("parallel"),
    )(o_fw, o_bw, p, p, onw)


def _gate_window_specs(col0):
    return [_rowcol(HW, col0 // HW), _rowcol(HW, col0 // HW + 1), _rowcol(DH, (col0 + 2 * HW) // DH)]


def _gate_window(refs):
    return jnp.concatenate([r[...].astype(F32) for r in refs], axis=1)


def _merge_fwd(p, u1, u2):
    t = p.shape[0]

    def body(a0, a1, a2, b0, b1, b2, u1_ref, u2_ref, m_ref):
        f = lambda r: r[...].astype(F32)
        m_ref[...] = (_sig(_gate_window((a0, a1, a2))) * f(u1_ref)
                      + _sig(_gate_window((b0, b1, b2))) * f(u2_ref)).astype(BF16)

    return pl.pallas_call(
        body, name="merge_fwd", grid=(t // TR,),
        in_specs=_gate_window_specs(GATE_HG0) + _gate_window_specs(GATE_GLA0) + [_row(GW), _row(GW)],
        out_specs=_row(GW),
        out_shape=jax.ShapeDtypeStruct((t, GW), BF16),
        compiler_params=_cp("parallel"),
    )(p, p, p, p, p, p, u1, u2)


def _mid_fwd(z, y1, nw_post, nw_pre, modc, modx):
    t = y1.shape[0]

    def body(zc_ref, zx_ref, y_ref, wpo_ref, wpr_ref, mc_ref, mx_ref, z1_ref, h_ref):
        is_ctx = pl.program_id(0) < NCT
        y = y_ref[...].astype(F32)
        z1 = _z_tile(zc_ref, zx_ref, is_ctx) + _mod_row(mc_ref, mx_ref, 2, is_ctx) * (y * _rstd(y) * wpo_ref[...])
        z1_ref[...] = z1
        n = z1 * _rstd(z1) * wpr_ref[...]
        h = n * (1.0 + _mod_row(mc_ref, mx_ref, 4, is_ctx)) + _mod_row(mc_ref, mx_ref, 3, is_ctx)
        h_ref[...] = h.astype(BF16)

    return pl.pallas_call(
        body, name="mid_fwd", grid=(t // TR,),
        in_specs=_z_specs() + [_row(D), _full((1, D)), _full((1, D)), _full((8, D)), _full((8, D))],
        out_specs=[_row(D), _row(D)],
        out_shape=[jax.ShapeDtypeStruct((t, D), F32), jax.ShapeDtypeStruct((t, D), BF16)],
        compiler_params=_cp("parallel"),
    )(*z, y1, nw_post, nw_pre, modc, modx)


def _swiglu_fwd(uv):
    t = uv.shape[0]

    def body(u_ref, v_ref, a_ref):
        a_ref[...] = (_silu(u_ref[...].astype(F32)) * v_ref[...].astype(F32)).astype(BF16)

    return pl.pallas_call(
        body, name="swiglu_fwd", grid=(t // TR,),
        in_specs=[_rowcol(D_FF, 0), _rowcol(D_FF, 1)],
        out_specs=_row(D_FF),
        out_shape=jax.ShapeDtypeStruct((t, D_FF), BF16),
        compiler_params=_cp("parallel"),
    )(uv, uv)


def _swiglu_bwd(uv, da):
    t = uv.shape[0]

    def body(u_ref, v_ref, da_ref, d_ref):
        u = u_ref[...].astype(F32)
        d = da_ref[...].astype(F32)
        d_ref[:, :D_FF] = (d * v_ref[...].astype(F32) * _dsilu(u)).astype(BF16)
        d_ref[:, D_FF:] = (d * _silu(u)).astype(BF16)

    return pl.pallas_call(
        body, name="swiglu_bwd", grid=(t // TR,),
        in_specs=[_rowcol(D_FF, 0), _rowcol(D_FF, 1), _row(D_FF)],
        out_specs=_row(2 * D_FF),
        out_shape=jax.ShapeDtypeStruct((t, 2 * D_FF), BF16),
        compiler_params=_cp("parallel"),
    )(uv, uv, da)


def _final(z1, y2, target, nw, modc, modx):
    t = z1.shape[0]

    def body(z1_ref, y_ref, tg_ref, w_ref, mc_ref, mx_ref, dz_ref, dy_ref, loss_ref, sm_ref):
        i = pl.program_id(0)
        is_ctx = i < NCT

        @pl.when(i == 0)
        def _():
            loss_ref[...] = jnp.zeros_like(loss_ref)
            sm_ref[...] = jnp.zeros_like(sm_ref)

        g = _mod_row(mc_ref, mx_ref, 5, is_ctx)
        y = y_ref[...].astype(F32)
        r = _rstd(y)
        w = w_ref[...]
        yr = y * r
        n = yr * w
        e = z1_ref[...] + g * n - tg_ref[...]
        lat = jnp.where(is_ctx, 0.0, 1.0)
        loss_ref[...] += lat * _colsum(e * e)
        dz = e * (lat / D)
        dz_ref[...] = dz
        _acc_mod(sm_ref, 0, is_ctx, _colsum(dz * n))
        dn = dz * g
        _acc_row(sm_ref, 2, _colsum(dn * yr))
        dy_ref[...] = _rms_bwd(dn * w, y, r).astype(BF16)

    return pl.pallas_call(
        body, name="final", grid=(t // TR,),
        in_specs=[_row(D), _row(D), pl.BlockSpec((TR, D), lambda i: (jnp.maximum(i - NCT, 0), 0)),
                  _full((1, D)), _full((8, D)), _full((8, D))],
        out_specs=[_row(D), _row(D), _full((1, D)), _full((8, D))],
        out_shape=[jax.ShapeDtypeStruct((t, D), F32), jax.ShapeDtypeStruct((t, D), BF16),
                   jax.ShapeDtypeStruct((1, D), F32), jax.ShapeDtypeStruct((8, D), F32)],
        compiler_params=_cp("arbitrary"),
    )(z1, y2, target, nw, modc, modx)


def _mid_bwd(dh2, dz, z1, y1, nw_post, nw_pre, modc, modx):
    t = z1.shape[0]

    def body(dh_ref, dz_ref, z1_ref, y_ref, wpo_ref, wpr_ref, mc_ref, mx_ref, dzo_ref, dy_ref, sm_ref):
        i = pl.program_id(0)
        is_ctx = i < NCT

        @pl.when(i == 0)
        def _():
            sm_ref[...] = jnp.zeros_like(sm_ref)

        dh = dh_ref[...].astype(F32)
        z1 = z1_ref[...]
        r = _rstd(z1)
        zr = z1 * r
        wpr = wpr_ref[...]
        n = zr * wpr
        _acc_mod(sm_ref, 0, is_ctx, _colsum(dh))
        _acc_mod(sm_ref, 2, is_ctx, _colsum(dh * n))
        dn = dh * (1.0 + _mod_row(mc_ref, mx_ref, 4, is_ctx))
        _acc_row(sm_ref, 6, _colsum(dn * zr))
        dz1 = dz_ref[...] + _rms_bwd(dn * wpr, z1, r)
        dzo_ref[...] = dz1
        y = y_ref[...].astype(F32)
        r1 = _rstd(y)
        yr = y * r1
        wpo = wpo_ref[...]
        g = _mod_row(mc_ref, mx_ref, 2, is_ctx)
        _acc_mod(sm_ref, 4, is_ctx, _colsum(dz1 * (yr * wpo)))
        dn1 = dz1 * g
        _acc_row(sm_ref, 7, _colsum(dn1 * yr))
        dy_ref[...] = _rms_bwd(dn1 * wpo, y, r1).astype(BF16)

    return pl.pallas_call(
        body, name="mid_bwd", grid=(t // TR,),
        in_specs=[_row(D)] * 4 + [_full((1, D)), _full((1, D)), _full((8, D)), _full((8, D))],
        out_specs=[_row(D), _row(D), _full((8, D))],
        out_shape=[jax.ShapeDtypeStruct((t, D), F32), jax.ShapeDtypeStruct((t, D), BF16),
                   jax.ShapeDtypeStruct((8, D), F32)],
        compiler_params=_cp("arbitrary"),
    )(dh2, dz, z1, y1, nw_post, nw_pre, modc, modx)


def _pre_bwd(dh1, dz, z, nw, modc, modx):
    t = dh1.shape[0]

    def body(dh_ref, dz_ref, zc_ref, zx_ref, w_ref, mc_ref, mx_ref, dzo_ref, sm_ref):
        i = pl.program_id(0)
        is_ctx = i < NCT

        @pl.when(i == 0)
        def _():
            sm_ref[...] = jnp.zeros_like(sm_ref)

        dh = dh_ref[...].astype(F32)
        x = _z_tile(zc_ref, zx_ref, is_ctx)
        r = _rstd(x)
        xr = x * r
        w = w_ref[...]
        _acc_mod(sm_ref, 0, is_ctx, _colsum(dh))
        _acc_mod(sm_ref, 2, is_ctx, _colsum(dh * (xr * w)))
        dn = dh * (1.0 + _mod_row(mc_ref, mx_ref, 1, is_ctx))
        _acc_row(sm_ref, 4, _colsum(dn * xr))
        dzo_ref[...] = dz_ref[...] + _rms_bwd(dn * w, x, r)

    return pl.pallas_call(
        body, name="pre_bwd", grid=(t // TR,),
        in_specs=[_row(D)] * 2 + _z_specs() + [_full((1, D)), _full((8, D)), _full((8, D))],
        out_specs=[pl.BlockSpec((TR, D), lambda i: (jnp.maximum(i - NCT, 0), 0)), _full((8, D))],
        out_shape=[jax.ShapeDtypeStruct((t - CTX, D), F32), jax.ShapeDtypeStruct((8, D), F32)],
        compiler_params=_cp("arbitrary"),
    )(dh1, dz, *z, nw, modc, modx)


def _merge_bwd(dm, p, u1, u2):
    t = dm.shape[0]

    def body(dm_ref, a0, a1, a2, b0, b1, b2, u1_ref, u2_ref, du1_ref, du2_ref, dg_ref):
        dm_ = dm_ref[...].astype(F32)
        s1 = _sig(_gate_window((a0, a1, a2)))
        s2 = _sig(_gate_window((b0, b1, b2)))
        du1_ref[...] = (dm_ * s1).astype(BF16)
        du2_ref[...] = (dm_ * s2).astype(BF16)
        dg_ref[:, :GW] = (dm_ * u1_ref[...].astype(F32) * s1 * (1.0 - s1)).astype(BF16)
        dg_ref[:, GW:] = (dm_ * u2_ref[...].astype(F32) * s2 * (1.0 - s2)).astype(BF16)

    return pl.pallas_call(
        body, name="merge_bwd", grid=(t // TR,),
        in_specs=[_row(GW)] + _gate_window_specs(GATE_HG0) + _gate_window_specs(GATE_GLA0) + [_row(GW), _row(GW)],
        out_specs=[_row(GW), _row(GW), _row(2 * GW)],
        out_shape=[jax.ShapeDtypeStruct((t, GW), BF16), jax.ShapeDtypeStruct((t, GW), BF16),
                   jax.ShapeDtypeStruct((t, 2 * GW), BF16)],
        compiler_params=_cp("parallel"),
    )(dm, p, p, p, p, p, p, u1, u2)


def _post_bwd(dy_hg, dy_gla, o_fw, o_bw, p, onw):
    t = o_fw.shape[0]

    def body(d1_ref, d2_ref, of_ref, ob_ref, g1_ref, g2_ref, w_ref, do_ref, dg_ref, sm_ref):
        @pl.when(pl.program_id(0) == 0)
        def _():
            sm_ref[...] = jnp.zeros_like(sm_ref)

        for h in range(NH):
            sl = slice(h * DH, (h + 1) * DH)
            gs = slice((h % (NH // 2)) * DH, (h % (NH // 2) + 1) * DH)
            g_ref, d_ref = (g1_ref, d1_ref) if h < NH // 2 else (g2_ref, d2_ref)
            o = of_ref[:, sl] + ob_ref[:, sl]
            r = _rstd(o)
            orr = o * r
            w = w_ref[:, sl]
            gt = g_ref[:, gs].astype(F32)
            dy = d_ref[:, gs].astype(F32)
            dg_ref[:, sl] = (dy * (orr * w) * _dsilu(gt)).astype(BF16)
            dn = dy * _silu(gt)
            sm_ref[0:1, sl] += _colsum(dn * orr)
            do_ref[:, sl] = _rms_bwd(dn * w, o, r)

    return pl.pallas_call(
        body, name="post_bwd", grid=(t // TR,),
        in_specs=[_row(HW), _row(HW), _row(D), _row(D), _rowcol(HW, MAIN0 // HW + 4), _rowcol(HW, MAIN0 // HW + 8),
                  _full((1, D))],
        out_specs=[_row(D), _row(D), _full((8, D))],
        out_shape=[jax.ShapeDtypeStruct((t, D), F32), jax.ShapeDtypeStruct((t, D), BF16),
                   jax.ShapeDtypeStruct((8, D), F32)],
        compiler_params=_cp("arbitrary"),
    )(dy_hg, dy_gla, o_fw, o_bw, p, p, onw)


def _gates_bwd(p, hg_lb, wgk, bgk, dgm, dgo, dq_f, dq_b, dv_f, dv_b, dk_f, dk_b, dg_f, dg_b):
    t = p.shape[0]
    seg = lambda j: _rowcol(HW, MAIN0 // HW + j)

    def body(hq_ref, hf_ref, hb_ref, lr_ref, lb_ref, wgk_ref, bgk_ref, dgm_ref, dgo_ref,
             dqf_ref, dqb_ref, dvf_ref, dvb_ref, dkf_ref, dkb_ref, dgf_ref, dgb_ref,
             dp_ref, dlb_ref, dw_ref, db_ref):
        @pl.when(pl.program_id(0) == 0)
        def _():
            dlb_ref[...] = jnp.zeros_like(dlb_ref)
            dw_ref[...] = jnp.zeros_like(dw_ref)
            db_ref[...] = jnp.zeros_like(db_ref)

        c0 = MAIN0

        def put(j, val):
            dp_ref[:, c0 + j * HW:c0 + (j + 1) * HW] = val.astype(BF16)

        dq = dqf_ref[...].astype(F32) + dqb_ref[...].astype(F32)
        dv = dvf_ref[...].astype(F32) + dvb_ref[...].astype(F32)
        put(0, dq[:, :HW] * _dsilu(hq_ref[...].astype(F32)))
        put(1, dv[:, :HW])
        put(5, dq[:, HW:] * (DH ** -0.5))
        put(7, dv[:, HW:])
        put(6, dkf_ref[:, HW:].astype(F32) + dkb_ref[:, HW:].astype(F32))
        dp_ref[:, c0 + 4 * HW:c0 + 5 * HW] = dgo_ref[:, :HW]
        dp_ref[:, c0 + 8 * HW:c0 + 9 * HW] = dgo_ref[:, HW:]
        lr = lr_ref[...].astype(BF16)
        xg = _dot(lr, wgk_ref[...], NN) + bgk_ref[...]
        dxg = []
        for d, (raw_ref, dk_ref, dg_ref) in enumerate(((hf_ref, dkf_ref, dgf_ref), (hb_ref, dkb_ref, dgb_ref))):
            lbd = _hg_lb(lb_ref, d)
            s = _sig(raw_ref[...].astype(F32))
            f = lbd + (1.0 - lbd) * s
            df = dg_ref[:, :HW] / f - dk_ref[:, :HW].astype(F32)
            put(2 + d, df * (1.0 - lbd) * s * (1.0 - s))
            dlb_ref[d:d + 1, :] += _colsum(df * (1.0 - s)) * (lbd * (1.0 - lbd))
            dxg.append(dg_ref[:, HW:] * (1.0 / GLA_NORM) * _sig(-xg[:, d * HW:(d + 1) * HW]))
        dxg = jnp.concatenate(dxg, axis=1)
        db_ref[0:1, :] += _colsum(dxg)
        dxg_b = dxg.astype(BF16)
        dw_ref[...] += _dot(lr, dxg_b, TN)
        dlr = _dot(dxg_b, wgk_ref[...], NT)
        dp_ref[:, LR0:LR0 + DH] = (dlr + dgm_ref[:, :DH].astype(F32)).astype(BF16)
        dp_ref[:, LR0 + DH:GATE_GLA0] = dgm_ref[:, DH:D]
        dp_ref[:, GATE_GLA0:GATE_GLA0 + DH] = dgm_ref[:, D:GW] + dgm_ref[:, GW:GW + DH]
        dp_ref[:, GATE_GLA0 + DH:GATE_GLA0 + GW] = dgm_ref[:, GW + DH:]
        dp_ref[:, GATE_GLA0 + GW:] = jnp.zeros((TR, W_IN_COLS - GATE_GLA0 - GW), BF16)

    return pl.pallas_call(
        body, name="gates_bwd", grid=(t // TR,),
        in_specs=[seg(0), seg(2), seg(3), _rowcol(DH, LR0 // DH), _full((2, 2, HW)), _full((DH, D)), _full((1, D)),
                  _row(2 * GW), _row(D)] + [_row(D)] * 8,
        out_specs=[_row(W_IN_COLS), _full((8, HW)), _full((DH, D)), _full((8, D))],
        out_shape=[jax.ShapeDtypeStruct((t, W_IN_COLS), BF16), jax.ShapeDtypeStruct((8, HW), F32),
                   jax.ShapeDtypeStruct((DH, D), F32), jax.ShapeDtypeStruct((8, D), F32)],
        compiler_params=_cp("arbitrary"),
    )(p, p, p, p, hg_lb, wgk, bgk, dgm, dgo, dq_f, dq_b, dv_f, dv_b, dk_f, dk_b, dg_f, dg_b)


def _scan_consts(rev):
    r = lax.broadcasted_iota(jnp.int32, (CHUNK, CHUNK), 0)
    u = lax.broadcasted_iota(jnp.int32, (CHUNK, CHUNK), 1)
    rp = lax.broadcasted_iota(jnp.int32, (CHUNK, 1), 0)
    if rev:
        r, u, rp = CHUNK - 1 - r, CHUNK - 1 - u, CHUNK - 1 - rp
    tri = jnp.where(u <= r, 1.0, 0.0).astype(F32)
    tri_t = jnp.where(r <= u, 1.0, 0.0).astype(F32)
    lv = []
    for b in LEVELS:
        sh = b.bit_length() - 1
        pair = ((r >> sh) == (u >> sh) + 1) & (((u >> sh) & 1) == 0)
        pair_t = ((u >> sh) == (r >> sh) + 1) & (((r >> sh) & 1) == 0)
        tside = ((rp >> sh) & 1) == 1
        lv.append((pair, pair_t, tside, jnp.where(tside, 1.0, -1.0).astype(F32)))
    bd = LEVELS[-1].bit_length() - 1
    diag = ((r >> bd) == (u >> bd)) & (u <= r)
    diag_t = ((r >> bd) == (u >> bd)) & (r <= u)
    return tri, tri_t, lv, diag, diag_t


def _row_of(pos, rev):
    return CHUNK - 1 - pos if rev else pos


def _chunk_terms(cum, b_scr, consts, rev):
    _, _, lv, _, _ = consts
    terms = []
    for b, (_, _, _, sgn) in zip(LEVELS, lv):
        pieces = []
        for j in range(CHUNK // (2 * b)):
            row = _row_of(2 * b * j + b - 1, rev)
            pieces.append(jnp.broadcast_to(b_scr[row:row + 1, :], (2 * b, DH)))
        if rev:
            pieces = pieces[::-1]
        bnd = pieces[0] if len(pieces) == 1 else jnp.concatenate(pieces, axis=0)
        terms.append(jnp.exp((cum - bnd) * sgn))
    b = LEVELS[-1]
    pieces = []
    for j in range(CHUNK // b):
        if j == 0:
            pieces.append(jnp.zeros((b, DH), F32))
        else:
            row = _row_of(b * j - 1, rev)
            pieces.append(jnp.broadcast_to(b_scr[row:row + 1, :], (b, DH)))
    if rev:
        pieces = pieces[::-1]
    start = jnp.concatenate(pieces, axis=0)
    wq = jnp.exp(jnp.minimum(cum - start, 0.0))
    wk = jnp.exp(jnp.minimum(start - cum, EXP_CLAMP))
    terms.append((wq, wk))
    return terms


def _run_staged(units):
    live = list(units)
    while live:
        nxt = []
        for u in live:
            try:
                next(u)
                nxt.append(u)
            except StopIteration:
                pass
        live = nxt


SCAN_TB = 256
SCAN_CB = SCAN_TB // CHUNK


def _block_order(i, ntb, rev):
    nctx = CTX // SCAN_TB
    if not rev:
        return i
    return jnp.where(i < nctx, nctx - 1 - i, ntb - 1 - (i - nctx))


def _chunk_in_block(j, rev):
    return SCAN_CB - 1 - j if rev else j


def _scan_fwd(q, k, v, g, rev):
    t = q.shape[0]
    nc = t // CHUNK
    hpb = SCAN_HEADS_FWD

    def body(q_ref, k_ref, v_ref, g_ref, o_ref, st_ref, s_scr, b_scr):
        consts = _scan_consts(rev)
        _, _, lv, diag, _ = consts
        masks = [lvl[0] for lvl in lv] + [diag]

        @pl.when(pl.program_id(1) == 0)
        def _():
            s_scr[...] = jnp.zeros_like(s_scr)

        tri = consts[0]
        state = {hh: s_scr[hh] for hh in range(hpb)}

        def unit(hh, j):
            sl = slice(hh * DH, (hh + 1) * DH)
            c = _chunk_in_block(j, rev)
            rows = slice(c * CHUNK, (c + 1) * CHUNK)
            b_ref = b_scr.at[hh * SCAN_CB + j]
            qc, kc, vc, gc = q_ref[rows, sl], k_ref[rows, sl], v_ref[rows, sl], g_ref[rows, sl]
            cum = _split_dot(tri, gc)
            b_ref[...] = cum
            yield
            terms = _chunk_terms(cum, b_ref, consts, rev)
            qf, kf = qc.astype(F32), kc.astype(F32)
            xs = [(jnp.where(tside, qf, kf) * w).astype(BF16) for w, (_, _, tside, _) in zip(terms[:-1], lv)]
            qd, kd = (qf * terms[-1][0]).astype(BF16), (kf * terms[-1][1]).astype(BF16)
            tot = _colsum(gc)
            qe = (qf * jnp.exp(cum)).astype(BF16)
            ke = (kf * jnp.exp(tot - cum)).astype(BF16)
            vb = vc.astype(BF16)
            yield
            scs = [_dot(x, x, NT) for x in xs] + [_dot(qd, kd, NT)]
            kv = _dot(vb, ke, TN)
            yield
            a = jnp.zeros((CHUNK, CHUNK), F32)
            for sc, m in zip(scs, masks):
                a = a + jnp.where(m, sc, 0.0)
            o_intra = _dot(a.astype(BF16), vb, NN)
            yield
            st = state[hh]
            st_ref[hh, c] = st
            o_ref[rows, sl] = o_intra + _dot(qe, st.astype(BF16), NT)
            state[hh] = st * jnp.exp(tot) + kv
            yield

        _run_staged([unit(hh, j) for hh in range(hpb) for j in range(SCAN_CB)])
        for hh in range(hpb):
            s_scr[hh] = state[hh]

    ntb = t // SCAN_TB
    col = pl.BlockSpec((SCAN_TB, hpb * DH), lambda h, i: (_block_order(i, ntb, rev), h))
    return pl.pallas_call(
        body, name="scan_fwd_" + ("bw" if rev else "fw"), grid=(NH // hpb, ntb),
        in_specs=[col] * 4,
        out_specs=[col, pl.BlockSpec((hpb, SCAN_CB, DH, DH), lambda h, i: (h, _block_order(i, ntb, rev), 0, 0))],
        out_shape=[jax.ShapeDtypeStruct((t, D), F32), jax.ShapeDtypeStruct((NH, nc, DH, DH), F32)],
        scratch_shapes=[pltpu.VMEM((hpb, DH, DH), F32), pltpu.VMEM((hpb * SCAN_CB, CHUNK, DH), F32)],
        compiler_params=_cp("parallel", "arbitrary"),
    )(q, k, v, g)


def _scan_bwd(q, k, v, g, do, states, rev):
    t = q.shape[0]
    nc = t // CHUNK
    hpb = SCAN_HEADS_BWD

    def body(q_ref, k_ref, v_ref, g_ref, do_ref, st_ref, dq_ref, dk_ref, dv_ref, dg_ref, ds_scr, b_scr):
        consts = _scan_consts(rev)
        _, tri_t, lv, diag, diag_t = consts
        masks = [(lvl[0], lvl[1]) for lvl in lv] + [(diag, diag_t)]
        @pl.when(pl.program_id(1) == 0)
        def _():
            ds_scr[...] = jnp.zeros_like(ds_scr)

        tri = consts[0]
        dstate = {hh: ds_scr[hh] for hh in range(hpb)}

        def unit(hh, jj):
            sl = slice(hh * DH, (hh + 1) * DH)
            c = _chunk_in_block(SCAN_CB - 1 - jj, rev)
            rows = slice(c * CHUNK, (c + 1) * CHUNK)
            b_ref = b_scr.at[hh * SCAN_CB + jj]
            qc, kc, vc, gc = q_ref[rows, sl], k_ref[rows, sl], v_ref[rows, sl], g_ref[rows, sl]
            dob = do_ref[rows, sl].astype(BF16)
            vb = vc.astype(BF16)
            cum = _split_dot(tri, gc)
            b_ref[...] = cum
            da = _dot(dob, vb, NT)
            da_t = _dot(vb, dob, NT)
            yield
            terms = _chunk_terms(cum, b_ref, consts, rev)
            qf, kf = qc.astype(F32), kc.astype(F32)
            xs = [(jnp.where(tside, qf, kf) * w).astype(BF16) for w, (_, _, tside, _) in zip(terms[:-1], lv)]
            wqd, wkd = terms[-1]
            qdb, kdb = (qf * wqd).astype(BF16), (kf * wkd).astype(BF16)
            tot = _colsum(gc)
            e_tot = jnp.exp(tot)
            e_b = jnp.exp(cum)
            e_t = jnp.exp(tot - cum)
            qeb = (qf * e_b).astype(BF16)
            keb = (kf * e_t).astype(BF16)
            dsym = [(jnp.where(m, da, 0.0) + jnp.where(m_t, da_t, 0.0)).astype(BF16) for m, m_t in masks[:-1]]
            dad = (jnp.where(diag, da, 0.0).astype(BF16), jnp.where(diag_t, da_t, 0.0).astype(BF16))
            yield
            sym = [_dot(x, x, NT) for x in xs]
            dxs = [_dot(d, x, NN) for d, x in zip(dsym, xs)]
            at_d = _dot(kdb, qdb, NT)
            dqt_d = _dot(dad[0], kdb, NN)
            dkt_d = _dot(dad[1], qdb, NN)
            qd = _dot(dob, qeb, TN)
            yield
            a_t = jnp.where(diag_t, at_d, 0.0)
            dq = dqt_d * wqd
            dk = dkt_d * wkd
            db = dqt_d * qdb.astype(F32) - dkt_d * kdb.astype(F32)
            for s, dx, x, w, (_, m_t, tside, sgn) in zip(sym, dxs, xs, terms[:-1], lv):
                a_t = a_t + jnp.where(m_t, s, 0.0)
                dxw = dx * w
                dq = dq + jnp.where(tside, dxw, 0.0)
                dk = dk + jnp.where(tside, 0.0, dxw)
                db = db + (dx * x.astype(F32)) * sgn
            dv_intra = _dot(a_t.astype(BF16), dob, NN)
            st = st_ref[hh, c]
            stb = st.astype(BF16)
            dqe = _dot(dob, stb, NN)
            yield
            dst = dstate[hh]
            dstb = dst.astype(BF16)
            dstate[hh] = dst * e_tot + qd
            dv_ref[rows, sl] = (dv_intra + _dot(keb, dstb, NT)).astype(BF16)
            dke = _dot(vb, dstb, NN)
            yield
            qe = qeb.astype(F32)
            ke = keb.astype(F32)
            dq_ref[rows, sl] = (dq + dqe * e_b).astype(BF16)
            dk_ref[rows, sl] = (dk + dke * e_t).astype(BF16)
            db = db + dqe * qe - dke * ke
            dtot = _colsum(dstb.astype(F32) * stb.astype(F32)) * e_tot + _colsum(dke * ke)
            dg_ref[rows, sl] = _split_dot(tri_t, db) + dtot
            yield

        _run_staged([unit(hh, jj) for hh in range(hpb) for jj in range(SCAN_CB)])
        for hh in range(hpb):
            ds_scr[hh] = dstate[hh]

    ntb = t // SCAN_TB
    blk = lambda i: _block_order(ntb - 1 - i, ntb, rev)
    col = pl.BlockSpec((SCAN_TB, hpb * DH), lambda h, i: (blk(i), h))
    out = jax.ShapeDtypeStruct((t, D), F32)
    outb = jax.ShapeDtypeStruct((t, D), BF16)
    return pl.pallas_call(
        body, name="scan_bwd_" + ("bw" if rev else "fw"), grid=(NH // hpb, ntb),
        in_specs=[col] * 5 + [pl.BlockSpec((hpb, SCAN_CB, DH, DH), lambda h, i: (h, blk(i), 0, 0))],
        out_specs=[col] * 4,
        out_shape=[outb] * 3 + [out],
        scratch_shapes=[pltpu.VMEM((hpb, DH, DH), F32), pltpu.VMEM((hpb * SCAN_CB, CHUNK, DH), F32)],
        compiler_params=_cp("parallel", "arbitrary"),
    )(q, k, v, g, do, states)


W_IN_GRAD_CHUNKS = (("a", (0, 512)), ("b", (0, 128)), ("b", (128, 512)))
W_IN_REF = 6688


def _layout_w_in(w):
    return jnp.pad(w, ((0, 0), (0, W_IN_COLS - W_IN_REF)))


def _unlayout_w_in(d):
    return d[:, :W_IN_REF]


W_IN_PAD = 896


def _assemble_w_in(g):
    n, r, wp = g.shape
    tr = 256
    tiles = wp // DH

    def body(g_ref, o_ref):
        lane = lax.broadcasted_iota(jnp.int32, (tr, DH), 1)
        for t in range(W_IN_COLS // DH):
            acc = None
            for j in range(n):
                c = DH * t - W_IN_SHARD * j
                if c <= -DH or c >= W_IN_SHARD:
                    continue
                k, s = divmod(c, DH)
                lo = g_ref[j, :, k * DH:(k + 1) * DH] if 0 <= k < tiles else None
                hi = g_ref[j, :, (k + 1) * DH:(k + 2) * DH] if s and 0 <= k + 1 < tiles else None
                if s:
                    zero = jnp.zeros((tr, DH), g.dtype)
                    lo = zero if lo is None else pltpu.roll(lo, DH - s, 1)
                    hi = zero if hi is None else pltpu.roll(hi, DH - s, 1)
                    part = jnp.where(lane < DH - s, lo, hi)
                else:
                    part = lo
                acc = part if acc is None else acc + part
            o_ref[:, t * DH:(t + 1) * DH] = jnp.zeros((tr, DH), g.dtype) if acc is None else acc

    return pl.pallas_call(
        body, name="assemble_w_in", grid=(r // tr,),
        in_specs=[pl.BlockSpec((n, tr, wp), lambda i: (0, i, 0))],
        out_specs=pl.BlockSpec((tr, W_IN_COLS), lambda i: (i, 0)),
        out_shape=jax.ShapeDtypeStruct((r, W_IN_COLS), g.dtype),
        compiler_params=_cp("parallel"),
    )(g)


def _gate_cols(w):
    return jnp.pad(w, ((0, 0), (GOFF, GW - GOFF - D)))


def _gate_rows(w):
    return jnp.pad(w, ((GOFF, GW - GOFF - D), (0, 0)))


def _layout_wgk(w):
    r = w.shape[1]
    top = jnp.concatenate([w[0], jnp.zeros_like(w[0])], axis=1)
    bot = jnp.concatenate([jnp.zeros_like(w[1]), w[1]], axis=1)
    return jnp.concatenate([top, bot, jnp.zeros((DH - 2 * r, D), w.dtype)], axis=0)


def _unlayout_wgk(d, r=16):
    return jnp.stack([d[:r, :HW], d[r:2 * r, HW:]])


def _local_step(z, target, modc, modx, norms, onw, hg_lb, wgk, bgk, get_w_in, get_mix, get_ffn, send):
    n_pre1, n_post1, n_pre2, n_post2 = norms
    t = z[0].shape[0] + z[1].shape[0]
    tm = 1152 if t % 1152 == 0 else 256
    h1 = _prenorm(z, n_pre1, modc, modx, 0, 1, "prenorm1")
    w_in = get_w_in(h1)
    p = _matmul(h1, w_in, NN, BF16, "mm_in", t, 1024, D)
    q, v, k_f, k_b, g_f, g_b = _gates_fwd(p, hg_lb, wgk, bgk)
    o_f, st_f = _scan_fwd(q, k_f, v, g_f, False)
    o_b, st_b = _scan_fwd(q, k_b, v, g_b, True)
    y = _post_fwd(o_f, o_b, p, onw)
    w_br_hg, w_br_gla, w_out = get_mix(y)
    u1 = _matmul(y, w_br_hg, NN, BF16, "mm_br_hg", tm, GW, HW, a_off=0)
    u2 = _matmul(y, w_br_gla, NN, BF16, "mm_br_gla", tm, GW, HW, a_off=1)
    merged = _merge_fwd(p, u1, u2)
    y1 = _matmul(merged, w_out, NN, BF16, "mm_out", tm, 512, GW)
    z1, h2 = _mid_fwd(z, y1, n_post1, n_pre2, modc, modx)
    w_gu_t, w_down = get_ffn(h2)
    uv = _matmul(h2, w_gu_t, NT, BF16, "mm_gu", t, D_FF // 2, D)
    act = _swiglu_fwd(uv)
    y2 = _matmul(act, w_down, NN, BF16, "mm_down", t, 512, D_FF)
    dz, dy2, loss_vec, sm_final = _final(z1, y2, target, n_post2, modc, modx)
    dact = _matmul(dy2, w_down, NT, BF16, "mm_down_dx", t, D_FF // 2, D)
    d_w_down = _matmul(act, dy2, TN, BF16, "mm_down_dw", D_FF // 2, 1024, t)
    duv = _swiglu_bwd(uv, dact)
    dh2 = _matmul(duv, w_gu_t, NN, BF16, "mm_gu_dx", tm, 512, D_FF)
    d_w_gate_t = _matmul(duv, h2, TN, BF16, "mm_gate_dw", D_FF // 2, 1024, t, a_off=0, m_out=D_FF)
    d_w_up_t = _matmul(duv, h2, TN, BF16, "mm_up_dw", D_FF // 2, 1024, t, a_off=2, m_out=D_FF)
    dh2 = send(("w_down", "w_gate_t", "w_up_t"), (d_w_down, d_w_gate_t, d_w_up_t), dh2)
    dz, dy1, sm_mid = _mid_bwd(dh2, dz, z1, y1, n_post1, n_pre2, modc, modx)
    dmerged = _matmul(dy1, w_out, NT, BF16, "mm_out_dx", tm, GW, D)
    d_w_out = _matmul(merged, dy1, TN, BF16, "mm_out_dw", GW, 512, t)
    du1, du2, dgm = _merge_bwd(dmerged, p, u1, u2)
    dy_hg = _matmul(du1, w_br_hg, NT, BF16, "mm_br_hg_dx", tm, HW, GW)
    dy_gla = _matmul(du2, w_br_gla, NT, BF16, "mm_br_gla_dx", tm, HW, GW)
    d_w_br_hg = _matmul(y, du1, TN, BF16, "mm_br_hg_dw", HW, GW, t, a_off=0, m_out=HW)
    d_w_br_gla = _matmul(y, du2, TN, BF16, "mm_br_gla_dw", HW, GW, t, a_off=1, m_out=HW)
    dy_hg = send(("w_out", "w_br_hg", "w_br_gla"), (d_w_out, d_w_br_hg, d_w_br_gla), dy_hg)
    do, dgo, sm_post = _post_bwd(dy_hg, dy_gla, o_f, o_b, p, onw)
    dq_f, dk_f, dv_f, dg_f = _scan_bwd(q, k_f, v, g_f, do, st_f, False)
    dq_b, dk_b, dv_b, dg_b = _scan_bwd(q, k_b, v, g_b, do, st_b, True)
    dp, d_lb, d_wgk, d_bgk = _gates_bwd(p, hg_lb, wgk, bgk, dgm, dgo, dq_f, dq_b, dv_f, dv_b, dk_f, dk_b, dg_f, dg_b)
    d_w_in_a = _matmul(h1, dp, TN, BF16, "mm_in_dw_a", 512, 1024, t, a_off=0, m_out=D // 2)
    dp = send(("w_in_a",), (d_w_in_a,), dp)
    d_w_in_b = _matmul(h1, dp, TN, BF16, "mm_in_dw_b", 512, 1024, t, a_off=1, m_out=D // 2)
    dp = send(("w_in_b",), (d_w_in_b,), dp)
    dh1 = _matmul(dp, w_in, NT, BF16, "mm_in_dx", tm, 512, W_IN_COLS // 2)
    grad_x, sm_pre = _pre_bwd(dh1, dz, z, n_pre1, modc, modx)
    return dict(loss_vec=loss_vec, grad_x=grad_x, sm_final=sm_final, sm_mid=sm_mid, sm_post=sm_post, sm_pre=sm_pre,
                d_lb=d_lb, d_wgk=d_wgk, d_bgk=d_bgk)


MESH = pl.DeviceIdType.MESH
ANY = pl.BlockSpec(memory_space=pl.ANY)
N_REL = N_DEV - 1


def _place():
    return lax.axis_index("x"), lax.axis_index("y"), lax.axis_index("c")


def _slot(p):
    return 4 * p[0] + 2 * p[1] + p[2]


def _all_gather(arrays, name):
    n = len(arrays)

    def body(*refs):
        ins, outs = refs[:n], refs[n:2 * n]
        send_sems, recv_sems, local_sems = refs[2 * n:]
        x, y, c = _place()
        me, sibling = (x, y, c), (x, y, 1 - c)
        chips = [(1 - x, y), (x, 1 - y), (1 - x, 1 - y)]

        def copy(a, k, block, to, src=None):
            dst = outs[a].at[_slot(block)]
            return pltpu.make_async_remote_copy(
                src_ref=dst if src is None else src, dst_ref=dst,
                send_sem=send_sems.at[N_REL * a + k], recv_sem=recv_sems.at[N_REL * a + k],
                device_id=to, device_id_type=MESH)

        mine = [pltpu.make_async_copy(ins[a], outs[a].at[_slot(me)], local_sems.at[a]) for a in range(n)]
        for cp in mine:
            cp.start()
        first = []
        for a in range(n):
            first.append(copy(a, 0, me, sibling, src=ins[a]))
            first += [copy(a, 1 + j, me, (*chip, c), src=ins[a]) for j, chip in enumerate(chips)]
        for cp in first:
            cp.start()
        passed = []
        for j, chip in enumerate(chips):
            for a in range(n):
                copy(a, 1 + j, (*chip, c), me).wait_recv()
                fwd = copy(a, 4 + j, (*chip, c), sibling)
                fwd.start()
                passed.append(fwd)
        for a in range(n):
            copy(a, 0, sibling, me).wait_recv()
        for j, chip in enumerate(chips):
            for a in range(n):
                copy(a, 4 + j, (*chip, 1 - c), me).wait_recv()
        for cp in first + passed:
            cp.wait_send()
        for cp in mine:
            cp.wait()

    return pl.pallas_call(
        body, name=name,
        in_specs=[ANY] * n, out_specs=[ANY] * n,
        out_shape=[jax.ShapeDtypeStruct((N_DEV,) + a.shape, a.dtype) for a in arrays],
        scratch_shapes=[pltpu.SemaphoreType.DMA((N_REL * n,)), pltpu.SemaphoreType.DMA((N_REL * n,)),
                        pltpu.SemaphoreType.DMA((n,))],
    )(*arrays)


def _exchange(arrays, name):
    n = len(arrays)

    def body(*refs):
        ins, outs = refs[:n], refs[n:2 * n]
        send_sems, recv_sems, local_sems = refs[2 * n:]
        x, y, c = _place()
        me = _slot((x, y, c))
        mine = [pltpu.make_async_copy(ins[a].at[me], outs[a].at[me], local_sems.at[a]) for a in range(n)]
        for cp in mine:
            cp.start()
        copies = []
        for a in range(n):
            for k in range(1, N_DEV):
                flip = lambda v, bit: 1 - v if bit else v
                peer = (flip(x, k & 4), flip(y, k & 2), flip(c, k & 1))
                copies.append(pltpu.make_async_remote_copy(
                    src_ref=ins[a].at[_slot(peer)], dst_ref=outs[a].at[me],
                    send_sem=send_sems.at[N_REL * a + k - 1], recv_sem=recv_sems.at[N_REL * a + k - 1],
                    device_id=peer, device_id_type=MESH))
                copies[-1].start()
        i = 0
        for a in range(n):
            for k in range(1, N_DEV):
                flip = lambda v, bit: 1 - v if bit else v
                peer = (flip(x, k & 4), flip(y, k & 2), flip(c, k & 1))
                pltpu.make_async_remote_copy(
                    src_ref=ins[a].at[_slot(peer)], dst_ref=outs[a].at[_slot(peer)],
                    send_sem=send_sems.at[N_REL * a + k - 1], recv_sem=recv_sems.at[N_REL * a + k - 1],
                    device_id=peer, device_id_type=MESH).wait_recv()
                i += 1
        for cp in copies:
            cp.wait_send()
        for cp in mine:
            cp.wait()

    return pl.pallas_call(
        body, name=name,
        in_specs=[ANY] * n, out_specs=[ANY] * n,
        out_shape=[jax.ShapeDtypeStruct(a.shape, a.dtype) for a in arrays],
        scratch_shapes=[pltpu.SemaphoreType.DMA((N_REL * n,)), pltpu.SemaphoreType.DMA((N_REL * n,)),
                        pltpu.SemaphoreType.DMA((n,))],
    )(*arrays)


HBM = pl.BlockSpec(memory_space=pltpu.HBM)
SEM = pl.BlockSpec(memory_space=pltpu.SEMAPHORE)
EFFECT = pltpu.SideEffectType.DATAFLOW_SIDE_EFFECTING


def _peer_of(x, y, c, k):
    flip = lambda v, bit: 1 - v if bit else v
    return flip(x, k & 4), flip(y, k & 2), flip(c, k & 1)


def _view_whole(src, slot):
    return src


def _view_near(src, slot):
    return src


_view_near.peers = (1, 2, 4, 6)


def _view_block(src, slot):
    return src.at[slot]


W_IN_SHARD = W_IN_REF // N_DEV


def _view_window(rows):
    def view(src, slot):
        col0 = pl.multiple_of((W_IN_SHARD * slot // DH) * DH, DH)
        return src.at[pl.ds(rows[0], rows[1] - rows[0]), pl.ds(col0, D)]
    return view


def _split_copies(view, srcs, lands, send_sems, recv_sems, local_sems):
    x, y, c = _place()
    me = _slot((x, y, c))
    local, sends, waits = [], [], []
    for a, (src, land) in enumerate(zip(srcs, lands)):
        local.append(pltpu.make_async_copy(view(src, me), land.at[me], local_sems.at[a]))
        for k in getattr(view, "peers", range(1, N_DEV)):
            peer = _peer_of(x, y, c, k)
            mine = view(src, _slot(peer))
            sems = dict(send_sem=send_sems.at[N_REL * a + k - 1], recv_sem=recv_sems.at[N_REL * a + k - 1],
                        device_id=peer, device_id_type=MESH)
            sends.append(pltpu.make_async_remote_copy(src_ref=mine, dst_ref=land.at[me], **sems))
            waits.append(pltpu.make_async_remote_copy(src_ref=mine, dst_ref=land.at[_slot(peer)], **sems))
    return local, sends, waits


def _split_start(view, land_shapes, srcs, name, after):
    n = len(srcs)
    lands = [lax.empty(shp, s.dtype) for shp, s in zip(land_shapes, srcs)]

    def body(*refs):
        src_refs, land_refs = refs[:n], refs[n:2 * n]
        send_sems, recv_sems, local_sems = refs[2 * n + 1:2 * n + 4]
        token = refs[-1]
        local, sends, _ = _split_copies(view, src_refs, land_refs, send_sems, recv_sems, local_sems)
        for cp in local + sends:
            cp.start()
        token[...] = jnp.zeros_like(token)

    hbm = lambda a: pltpu.with_memory_space_constraint(a, pltpu.HBM)
    out = pl.pallas_call(
        body, name=name,
        out_shape=(pltpu.SemaphoreType.DMA((N_REL * n,)), pltpu.SemaphoreType.DMA((N_REL * n,)),
                   pltpu.SemaphoreType.DMA((n,)),
                   *[pltpu.HBM(s.shape, s.dtype) for s in srcs], *[pltpu.HBM(l.shape, l.dtype) for l in lands],
                   jax.ShapeDtypeStruct((8, DH), F32)),
        in_specs=[HBM] * (2 * n) + [ANY],
        out_specs=(SEM, SEM, SEM, *([HBM] * (2 * n)), pl.BlockSpec(memory_space=pltpu.VMEM)),
        input_output_aliases={i: 3 + i for i in range(2 * n)},
        compiler_params=pltpu.CompilerParams(has_side_effects=EFFECT),
    )(*[hbm(s) for s in srcs], *[hbm(l) for l in lands], after)
    handle = dict(view=view, n=n, sems=out[:3], srcs=list(out[3:3 + n]), lands=list(out[3 + n:3 + 2 * n]))
    return handle, out[-1]


def _split_wait(handle, name, after, srcs=None):
    view, n, sems, lands = handle["view"], handle["n"], handle["sems"], handle["lands"]
    srcs = handle["srcs"] if srcs is None else srcs
    afters = list(after) if isinstance(after, (list, tuple)) else [after]

    def body(*refs):
        src_refs, land_refs = refs[:n], refs[n:2 * n]
        send_sems, recv_sems, local_sems = refs[2 * n:2 * n + 3]
        local, _, waits = _split_copies(view, src_refs, land_refs, send_sems, recv_sems, local_sems)
        for cp in waits:
            cp.wait_send()
            cp.wait_recv()
        for cp in local:
            cp.wait()

    out = pl.pallas_call(
        body, name=name,
        out_shape=(*[pltpu.HBM(s.shape, s.dtype) for s in srcs], *[pltpu.HBM(l.shape, l.dtype) for l in lands]),
        in_specs=[HBM] * (2 * n) + [SEM, SEM, SEM] + [ANY] * len(afters),
        out_specs=tuple([HBM] * (2 * n)),
        input_output_aliases={i: i for i in range(2 * n)},
        compiler_params=pltpu.CompilerParams(has_side_effects=EFFECT),
    )(*srcs, *lands, *sems, *afters)
    handle["srcs"] = list(out[:n])
    return list(out[n:])


def _tie(x, token, name):
    def body(x_ref, t_ref, o_ref):
        pass

    return pl.pallas_call(
        body, name=name, out_shape=jax.ShapeDtypeStruct(x.shape, x.dtype),
        in_specs=[ANY, ANY], out_specs=ANY, input_output_aliases={0: 0},
    )(x, token)


def _forward_to_sibling(land, name):
    def body(land_ref, out_ref, send_sems, recv_sems):
        x, y, c = _place()
        sibling = (x, y, 1 - c)
        chips = [(1 - x, y), (x, 1 - y), (1 - x, 1 - y)]

        def copy(j, core):
            blk = _slot((*chips[j], core))
            return pltpu.make_async_remote_copy(src_ref=land_ref.at[blk], dst_ref=out_ref.at[blk],
                                                send_sem=send_sems.at[j], recv_sem=recv_sems.at[j],
                                                device_id=sibling, device_id_type=MESH)

        sends = [copy(j, c) for j in range(3)]
        for cp in sends:
            cp.start()
        for j in range(3):
            copy(j, 1 - c).wait_recv()
        for cp in sends:
            cp.wait_send()

    return pl.pallas_call(
        body, name=name, in_specs=[ANY], out_specs=ANY, input_output_aliases={0: 0},
        out_shape=jax.ShapeDtypeStruct(land.shape, land.dtype),
        scratch_shapes=[pltpu.SemaphoreType.DMA((3,)), pltpu.SemaphoreType.DMA((3,))],
    )(land)


def _mod_fwd(a, w, b):
    def body(a_ref, w_ref, b_ref, o_ref):
        o_ref[...] = _dot(_silu(a_ref[...]), w_ref[...], NN, precision=HI) + b_ref[...]

    return pl.pallas_call(
        body, name="mod_fwd", out_shape=jax.ShapeDtypeStruct((a.shape[0], w.shape[1]), F32),
        compiler_params=pltpu.CompilerParams(vmem_limit_bytes=VMEM_LIMIT),
    )(a, w, b)


def _mod_bwd(a, d, w):
    def body(a_ref, d_ref, w_ref, dw_ref, dc_ref):
        av = a_ref[...]
        dv = d_ref[...]
        dw_ref[...] = _dot(_silu(av), dv, TN, precision=HI)
        da = _dot(dv[0:8, :], w_ref[...], NT, precision=HI) * _dsilu(av[0:8, :])
        row = lax.broadcasted_iota(jnp.int32, da.shape, 0)
        dc_ref[...] = jnp.where(row == 0, da, 0.0)

    return pl.pallas_call(
        body, name="mod_bwd",
        out_shape=[jax.ShapeDtypeStruct(w.shape, F32), jax.ShapeDtypeStruct((8, w.shape[0]), F32)],
        compiler_params=pltpu.CompilerParams(vmem_limit_bytes=VMEM_LIMIT),
    )(a, d, w)


def _sum_devices(g):
    def body(g_ref, o_ref):
        acc = g_ref[0]
        for i in range(1, g.shape[0]):
            acc = acc + g_ref[i]
        o_ref[...] = acc

    return pl.pallas_call(body, name="sum_devices_%d" % g.shape[1],
                          out_shape=jax.ShapeDtypeStruct(g.shape[1:], F32))(g)


def _sum_windows(g, name):
    n, r, c = g.shape
    tr = 128

    def body(g_ref, o_ref):
        x, y, cc = _place()
        lane0 = (W_IN_SHARD * _slot((x, y, cc))) % DH
        acc = g_ref[0].astype(F32)
        for i in range(1, n):
            acc = acc + g_ref[i].astype(F32)
        o_ref[...] = pltpu.roll(acc, (c - lane0) % c, 1).T

    return pl.pallas_call(
        body, name=name, grid=(r // tr,),
        in_specs=[pl.BlockSpec((n, tr, c), lambda i: (0, i, 0))],
        out_specs=pl.BlockSpec((c, tr), lambda i: (0, i)),
        out_shape=jax.ShapeDtypeStruct((c, r), F32),
        compiler_params=_cp("parallel"),
    )(g)


def _adam_rows(r, c, n):
    budget = 6 * 1024 * 1024
    best = None
    for tr in range(16, r + 1, 16):
        if r % tr == 0 and tr * c * (2 * n + 28) <= budget:
            best = tr
    return best if best is not None else r


def _adamw(g, w, m, v, name):
    n, r, c = g.shape
    tr = _adam_rows(r, c, n)
    bc1 = 1.0 - ADAM_B1 ** ADAM_STEP
    bc2 = 1.0 - ADAM_B2 ** ADAM_STEP

    def body(g_ref, w_ref, m_ref, v_ref, go_ref, d_ref, mo_ref, vo_ref):
        grad = g_ref[0].astype(F32)
        for i in range(1, n):
            grad = grad + g_ref[i].astype(F32)
        go_ref[...] = grad
        m_new = ADAM_B1 * m_ref[...] + (1.0 - ADAM_B1) * grad
        v_new = ADAM_B2 * v_ref[...] + (1.0 - ADAM_B2) * (grad * grad)
        mo_ref[...] = m_new
        vo_ref[...] = v_new
        d_ref[...] = -ADAM_LR * ((m_new / bc1) / (jnp.sqrt(v_new / bc2) + ADAM_EPS) + ADAM_WD * w_ref[...])

    blk = pl.BlockSpec((tr, c), lambda i: (i, 0))
    out = jax.ShapeDtypeStruct((r, c), F32)
    return pl.pallas_call(
        body, name=name, grid=(r // tr,),
        in_specs=[pl.BlockSpec((n, tr, c), lambda i: (0, i, 0)), blk, blk, blk],
        out_specs=[blk] * 4, out_shape=[out] * 4,
        compiler_params=_cp("parallel"),
    )(g, w, m, v)


ADAM_ROWS3 = 168


def _adam_math(grad, w, m, v):
    bc1 = 1.0 - ADAM_B1 ** ADAM_STEP
    bc2 = 1.0 - ADAM_B2 ** ADAM_STEP
    m_new = ADAM_B1 * m + (1.0 - ADAM_B1) * grad
    v_new = ADAM_B2 * v + (1.0 - ADAM_B2) * (grad * grad)
    delta = -ADAM_LR * ((m_new / bc1) / (jnp.sqrt(v_new / bc2) + ADAM_EPS) + ADAM_WD * w)
    return delta, m_new, v_new


def _adamw_rows3(g, w3, m3, v3, name):
    r, _, c = w3.shape
    n = ADAM_ROWS3
    starts = list(range(0, r - n, n)) + [r - n]

    def body(g_hbm, w_hbm, m_hbm, v_hbm, go_hbm, d_hbm, mo_hbm, vo_hbm, gbuf, ibuf, obuf, sems):
        for r0 in starts:
            g0 = (r0 // 8) * 8
            ins = [pltpu.make_async_copy(g_hbm.at[pl.ds(g0, n + 8)], gbuf, sems.at[0])]
            ins += [pltpu.make_async_copy(h.at[pl.ds(r0, n), 0], ibuf.at[k], sems.at[1 + k])
                    for k, h in enumerate((w_hbm, m_hbm, v_hbm))]
            for cp in ins:
                cp.start()
            for cp in ins:
                cp.wait()
            grad = gbuf[pl.ds(r0 - g0, n), :]
            delta, m_new, v_new = _adam_math(grad, ibuf[0], ibuf[1], ibuf[2])
            for k, val in enumerate((grad, delta, m_new, v_new)):
                obuf[k] = val
            outs = [pltpu.make_async_copy(obuf.at[k], h.at[pl.ds(r0, n), 0], sems.at[4 + k])
                    for k, h in enumerate((go_hbm, d_hbm, mo_hbm, vo_hbm))]
            for cp in outs:
                cp.start()
            for cp in outs:
                cp.wait()

    out = jax.ShapeDtypeStruct(w3.shape, F32)
    return pl.pallas_call(
        body, name=name, in_specs=[ANY] * 4, out_specs=[ANY] * 4, out_shape=[out] * 4,
        scratch_shapes=[pltpu.VMEM((n + 8, c), F32), pltpu.VMEM((3, n, c), F32), pltpu.VMEM((4, n, c), F32),
                        pltpu.SemaphoreType.DMA((8,))],
        compiler_params=pltpu.CompilerParams(vmem_limit_bytes=VMEM_LIMIT),
    )(g, w3, m3, v3)


def kernel(x, c, ctx, c_ctx, w_mod, b_mod, norm_pre1, norm_post1, norm_pre2, norm_post2, w_in, hg_lb, hg_onorm, gla_w_gk, gla_b_gk, gla_onorm, w_br_hg, w_br_gla, w_out, w_ff_gate, w_ff_up, w_ff_down, loss_target, m_c_ctx, m_w_mod, m_b_mod, m_norm_pre1, m_norm_post1, m_norm_pre2, m_norm_post2, m_w_in, m_hg_lb, m_hg_onorm, m_gla_w_gk, m_gla_b_gk, m_gla_onorm, m_w_br_hg, m_w_br_gla, m_w_out, m_w_ff_gate, m_w_ff_up, m_w_ff_down, v_c_ctx, v_w_mod, v_b_mod, v_norm_pre1, v_norm_post1, v_norm_pre2, v_norm_post2, v_w_in, v_hg_lb, v_hg_onorm, v_gla_w_gk, v_gla_b_gk, v_gla_onorm, v_w_br_hg, v_w_br_gla, v_w_out, v_w_ff_gate, v_w_ff_up, v_w_ff_down):
    xi, yi, ci = lax.axis_index("x"), lax.axis_index("y"), lax.axis_index("c")
    me = 4 * xi + 2 * yi + ci
    t = CTX + x.shape[1]

    c_all, lb_g, wgk_g, bgk_g = _all_gather([c, hg_lb, gla_w_gk[0], gla_b_gk[0]], "ag_small")
    tr_ = lambda a: jnp.swapaxes(a[0], 0, 1)
    big = [w_in[0], w_br_hg[0], w_br_gla[0], w_out[0], tr_(w_ff_gate), tr_(w_ff_up), w_ff_down[0]]
    big_bf = [w.astype(BF16) for w in big]
    big_bf[0] = jnp.pad(big_bf[0], ((0, 0), (0, W_IN_PAD - W_IN_SHARD)))
    cols = lambda g: jnp.transpose(g, (1, 0, 2)).reshape(g.shape[1], N_DEV * g.shape[2])

    def get_w_in(after):
        land, = _split_wait(w_in_handle, "ag_w_in_wait", after)
        return _assemble_w_in(_forward_to_sibling(land, "ag_w_in_forward"))

    def get_mix(after):
        g_brh, g_brg, g_out = _split_wait(mix_handle, "ag_mix_wait", after)
        return _gate_cols(cols(g_brh)), _gate_cols(cols(g_brg)), _gate_rows(g_out.reshape(D, D))

    def get_ffn(after):
        g_gate, g_up, g_down = _split_wait(ffn_handle, "ag_ffn_wait", after)
        return (g_gate.reshape(D_FF, D), g_up.reshape(D_FF, D)), g_down.reshape(D_FF, D)

    hg_lb_full = jnp.transpose(lb_g, (1, 2, 0, 3)).reshape(2, 2, HW)
    wgk_k = _layout_wgk(jnp.transpose(wgk_g, (1, 2, 0, 3)).reshape(2, 16, HW)).astype(BF16)
    bgk_k = jnp.transpose(bgk_g, (1, 0, 2)).reshape(1, D)
    onw = jnp.concatenate([jnp.tile(hg_onorm, (1, NH // 2)), jnp.tile(gla_onorm, (1, NH // 2))], axis=1)

    n_mod = w_mod.shape[2]
    a9 = jnp.concatenate([c_ctx[None], c_all[:, 0], jnp.zeros((16 - 1 - N_DEV, D), F32)], axis=0)
    b_loc = lax.dynamic_slice(b_mod, (0, me * n_mod), (1, n_mod))
    s_loc = _mod_fwd(a9, w_mod[0], b_loc)
    s_all, = _all_gather([s_loc], "ag_mod")
    mod_all = jnp.transpose(s_all, (1, 0, 2)).reshape(16, N_DEV * n_mod)
    pad8 = lambda m: jnp.concatenate([m.reshape(6, D), jnp.zeros((2, D), F32)], axis=0)
    modc = pad8(mod_all[0])
    modx = pad8(lax.dynamic_slice(mod_all, (1 + me, 0), (1, N_DEV * n_mod))[0])

    gathered = lambda arrs: [(N_DEV,) + a.shape for a in arrs]
    w_in_handle, tok = _split_start(_view_near, gathered(big_bf[:1]), big_bf[:1], "ag_w_in_start", s_all)
    mix_handle, tok = _split_start(_view_whole, gathered(big_bf[1:4]), big_bf[1:4], "ag_mix_start", tok)
    ffn_handle, tok = _split_start(_view_whole, gathered(big_bf[4:]), big_bf[4:], "ag_ffn_start", tok)

    z = (ctx[0], x[0])
    modx = _tie(modx, tok, "tie_mod")
    norms = (norm_pre1, norm_post1, norm_pre2, norm_post2)
    shard = lambda d: jnp.transpose(d.reshape(d.shape[0], N_DEV, -1), (1, 0, 2)).astype(BF16)
    rowshard = lambda d: d.reshape(N_DEV, d.shape[0] // N_DEV, d.shape[1]).astype(BF16)
    sent, w_in_grad = [], {}

    def send_w_in(i, x_after):
        half, rows = W_IN_GRAD_CHUNKS[i]
        handle, tok = _split_start(_view_window(rows), [(N_DEV, rows[1] - rows[0], D)], w_in_grad[half],
                                   "grads_w_in%d_start" % i, x_after)
        w_in_grad[half] = handle["srcs"]
        sent.append(("w_in%d" % i, ["w_in#%d" % i], handle))
        return _tie(x_after, tok, "tie_w_in%d" % i)

    def send(names, grads, x_after):
        if names == ("w_in_a",):
            w_in_grad["a"] = list(grads)
            return send_w_in(0, x_after)
        if names == ("w_in_b",):
            w_in_grad["b"] = list(grads)
            return x_after
        arrs, leaves = [], []
        for nm, g in zip(names, grads):
            if nm in ("w_gate_t", "w_up_t"):
                arrs.append(rowshard(g))
                leaves.append({"w_gate_t": "w_ff_gate", "w_up_t": "w_ff_up"}[nm])
            elif nm == "w_down":
                arrs.append(rowshard(g))
                leaves.append("w_ff_down")
            elif nm == "w_out":
                arrs.append(rowshard(g[GOFF:GOFF + D]))
                leaves.append(nm)
            else:
                arrs.append(shard(g[:, GOFF:GOFF + D]))
                leaves.append(nm)
        handle, tok = _split_start(_view_block, [a.shape for a in arrs], arrs, "grads_%s_start" % names[0], x_after)
        sent.append((names[0], leaves, handle))
        return _tie(x_after, tok, "tie_" + names[0])

    r = _local_step(z, loss_target[0], modc, modx, norms, onw, hg_lb_full, wgk_k, bgk_k,
                    get_w_in, get_mix, get_ffn, send)
    grad_x = r["grad_x"][None]

    sm_pre, sm_mid, sm_fin = r["sm_pre"], r["sm_mid"], r["sm_final"]
    dmodc = jnp.stack([sm_pre[0], sm_pre[2], sm_mid[4], sm_mid[0], sm_mid[2], sm_fin[0]]).reshape(-1)
    dmodx = jnp.stack([sm_pre[1], sm_pre[3], sm_mid[5], sm_mid[1], sm_mid[3], sm_fin[1]]).reshape(-1)
    on = r["sm_post"][0].reshape(NH, DH)
    pieces = [dmodc, dmodx, sm_pre[4], sm_mid[7], sm_mid[6], sm_fin[2], on[:NH // 2].sum(0), on[NH // 2:].sum(0),
              r["d_lb"][:2].reshape(-1), _unlayout_wgk(r["d_wgk"]).reshape(-1), r["d_bgk"][0]]
    loss_local = (0.5 / D) * jnp.sum(r["loss_vec"])
    pieces.append(jnp.concatenate([loss_local.reshape(1), jnp.zeros((DH - 1,), F32)]))
    sizes = [p.shape[0] for p in pieces]
    pack = jnp.concatenate(pieces).reshape(-1, DH)
    pack_all, = _all_gather([pack], "ag_small_grads")
    pack_all = send_w_in(1, pack_all)
    tot = _sum_devices(pack_all).reshape(-1)
    offs = [sum(sizes[:i]) for i in range(len(sizes))]
    part = lambda i: tot[offs[i]:offs[i] + sizes[i]]
    dmodc_t, dmodx_t = part(0), part(1)
    g_b_mod = (dmodc_t + dmodx_t)[None]
    g_norms = [part(i)[None] for i in (2, 3, 4, 5)]
    g_hg_on, g_gla_on = part(6)[None], part(7)[None]
    lb0 = lax.dynamic_slice(part(8).reshape(2, HW), (0, me * (HW // N_DEV)), (2, HW // N_DEV))
    g_hg_lb = jnp.stack([lb0, -lb0])
    g_wgk = lax.dynamic_slice(part(9).reshape(2, 16, HW), (0, 0, me * (HW // N_DEV)), (2, 16, HW // N_DEV))[None]
    g_bgk = lax.dynamic_slice(part(10).reshape(2, HW), (0, me * (HW // N_DEV)), (2, HW // N_DEV))[None]
    loss = part(11)[0]

    dmx_all = pack_all.reshape(N_DEV, -1)[:, sizes[0]:sizes[0] + sizes[1]]
    d9 = jnp.concatenate([lax.dynamic_slice(dmodc_t[None], (0, me * n_mod), (1, n_mod)),
                          lax.dynamic_slice(dmx_all, (0, me * n_mod), (N_DEV, n_mod)),
                          jnp.zeros((16 - 1 - N_DEV, n_mod), F32)], axis=0)
    g_w_mod, dcc_part = _mod_bwd(a9, d9, w_mod[0])
    dcc_all, = _all_gather([dcc_part], "ag_c_ctx")
    dcc_all = send_w_in(2, dcc_all)
    g_c_ctx = _sum_devices(dcc_all)[0]

    recv = {}
    for first, leaves, handle in sent:
        if not first.startswith("w_in"):
            recv.update(zip(leaves, _split_wait(handle, "grads_%s_wait" % first, g_c_ctx)))
    moms = [(m_w_in, v_w_in), (m_w_br_hg, v_w_br_hg), (m_w_br_gla, v_w_br_gla), (m_w_out, v_w_out),
            (m_w_ff_gate, v_w_ff_gate), (m_w_ff_up, v_w_ff_up), (m_w_ff_down, v_w_ff_down)]
    names = ["w_in", "w_br_hg", "w_br_gla", "w_out", "w_ff_gate", "w_ff_up", "w_ff_down"]
    res = {}

    def update(nm, w, m, v):
        if nm in ("w_ff_gate", "w_ff_up"):
            outs = _adamw(recv[nm], w, tr_(m), tr_(v), "adamw_" + nm)
            res[nm] = [jnp.swapaxes(o, 0, 1)[None] for o in outs]
        else:
            res[nm] = [o[None] for o in _adamw(recv[nm], w, m[0], v[0], "adamw_" + nm)]

    for nm, w, (m, v) in list(zip(names, big, moms))[1:]:
        update(nm, w, m, v)
    res["w_mod"] = [o[None] for o in _adamw(g_w_mod[None], w_mod[0], m_w_mod[0], v_w_mod[0], "adamw_w_mod")]

    small = [("c_ctx", c_ctx, m_c_ctx, v_c_ctx, g_c_ctx), ("b_mod", b_mod, m_b_mod, v_b_mod, g_b_mod),
             ("norm_pre1", norm_pre1, m_norm_pre1, v_norm_pre1, g_norms[0]),
             ("norm_post1", norm_post1, m_norm_post1, v_norm_post1, g_norms[1]),
             ("norm_pre2", norm_pre2, m_norm_pre2, v_norm_pre2, g_norms[2]),
             ("norm_post2", norm_post2, m_norm_post2, v_norm_post2, g_norms[3]),
             ("hg_lb", hg_lb, m_hg_lb, v_hg_lb, g_hg_lb), ("hg_onorm", hg_onorm, m_hg_onorm, v_hg_onorm, g_hg_on),
             ("gla_w_gk", gla_w_gk, m_gla_w_gk, v_gla_w_gk, g_wgk), ("gla_b_gk", gla_b_gk, m_gla_b_gk, v_gla_b_gk, g_bgk),
             ("gla_onorm", gla_onorm, m_gla_onorm, v_gla_onorm, g_gla_on)]
    flat = lambda k: jnp.concatenate([s[k].reshape(-1) for s in small]).reshape(-1, DH)
    outs = _adamw(flat(4)[None], flat(1), flat(2), flat(3), "adamw_small")
    off = 0
    for nm, w, _, _, _ in small:
        res[nm] = [o.reshape(-1)[off:off + w.size].reshape(w.shape) for o in outs]
        off += w.size

    done = [res[nm][0] for nm in names[1:]] + [res["w_mod"][0], outs[0]]
    sums = []
    for i, (first, leaves, handle) in enumerate(s for s in sent if s[0].startswith("w_in")):
        half = W_IN_GRAD_CHUNKS[i][0]
        land, = _split_wait(handle, "grads_%s_wait" % first, done, srcs=w_in_grad[half])
        w_in_grad[half] = handle["srcs"]
        sums.append(_sum_windows(land, "sum_windows%d" % i))
    major = lambda a: jnp.transpose(a, (2, 0, 1))
    outs = _adamw_rows3(jnp.concatenate(sums, axis=1), major(w_in), major(m_w_in), major(v_w_in), "adamw_w_in")
    res["w_in"] = [jnp.transpose(o, (1, 2, 0)) for o in outs]

    order = ["c_ctx", "w_mod", "b_mod", "norm_pre1", "norm_post1", "norm_pre2", "norm_post2", "w_in", "hg_lb",
             "hg_onorm", "gla_w_gk", "gla_b_gk", "gla_onorm", "w_br_hg", "w_br_gla", "w_out", "w_ff_gate", "w_ff_up",
             "w_ff_down"]
    return (loss, grad_x, *[res[n][k] for k in range(4) for n in order])
```

```python
import functools

import jax
import jax.numpy as jnp
from jax import lax
from jax.experimental import pallas as pl
from jax.experimental.pallas import tpu as pltpu

F32 = jnp.float32
BF16 = jnp.bfloat16
HI = lax.Precision.HIGHEST

N_DEV = 8
D = 1024
CTX = 256
HW = 512
DH = 128
NH = 8
D_FF = 2816
EPS = 1e-6
GLA_NORM = 16.0
CHUNK = 64
TR = 256
NCT = CTX // TR
W_IN_COLS = 7168
MAIN0 = 0
LR0 = 4608
GW = 1152
GOFF = 32
GATE_HG0 = LR0
GATE_GLA0 = LR0 + D
LEVELS = (32, 16, 8)
EXP_CLAMP = 80.0
VMEM_LIMIT = 48 * 1024 * 1024

ADAM_LR, ADAM_B1, ADAM_B2, ADAM_EPS, ADAM_WD, ADAM_STEP = 0.001, 0.9, 0.999, 1e-08, 0.01, 10


def _cp(*sem):
    return pltpu.CompilerParams(dimension_semantics=sem, vmem_limit_bytes=VMEM_LIMIT)


def _sig(x):
    return jax.nn.sigmoid(x)


def _silu(x):
    return x * _sig(x)


def _dsilu(x):
    s = _sig(x)
    return s * (1.0 + x * (1.0 - s))


def _rstd(x):
    return lax.rsqrt(jnp.mean(x * x, axis=-1, keepdims=True) + EPS)


def _rms_bwd(a, y, r):
    return r * (a - y * (r * r) * jnp.mean(a * y, axis=-1, keepdims=True))


def _colsum(x):
    return jnp.sum(x, axis=0, keepdims=True)


def _dot(a, b, dims, precision=None):
    return lax.dot_general(a, b, (dims, ((), ())), preferred_element_type=F32, precision=precision)


NN = ((1,), (0,))
NT = ((1,), (1,))
TN = ((0,), (0,))

SCAN_HEADS_FWD = 4
SCAN_HEADS_BWD = 4


def _split_dot(m, x):
    mb = m.astype(BF16)
    x1 = x.astype(BF16)
    r1 = x - x1.astype(F32)
    x2 = r1.astype(BF16)
    x3 = (r1 - x2.astype(F32)).astype(BF16)
    return _dot(mb, x1, NN) + _dot(mb, x2, NN) + _dot(mb, x3, NN)


def _matmul(a, b, dims, out_dtype, name, tm, tn, tk, a_off=0, m_out=None):
    pair = isinstance(b, (tuple, list))
    bs = list(b) if pair else [b]
    b1 = bs[0]
    rows = b1.shape[0] * len(bs)
    half = None
    if dims == NN:
        m, k, n = a.shape[0], rows, b1.shape[1]
        a_spec = pl.BlockSpec((tm, tk), lambda i, j, kk: (i, kk + a_off))
        half = b1.shape[0] // tk
        b_maps = [lambda i, j, kk: (kk, j)] if not pair else [
            lambda i, j, kk: (jnp.minimum(kk, half - 1), j), lambda i, j, kk: (jnp.maximum(kk - half, 0), j)]
        b_specs = [pl.BlockSpec((tk, tn), f) for f in b_maps]
        axis = 2
    elif dims == NT:
        m, k, n = a.shape[0], b1.shape[1], rows
        a_spec = pl.BlockSpec((tm, tk), lambda i, j, kk: (i, kk + a_off))
        half = b1.shape[0] // tn
        b_maps = [lambda i, j, kk: (j, kk)] if not pair else [
            lambda i, j, kk: (jnp.minimum(j, half - 1), kk), lambda i, j, kk: (jnp.maximum(j - half, 0), kk)]
        b_specs = [pl.BlockSpec((tn, tk), f) for f in b_maps]
        axis = 1
    else:
        assert not pair
        m, k = (a.shape[1] if m_out is None else m_out), a.shape[0]
        n = b1.shape[1]
        a_spec = pl.BlockSpec((tk, tm), lambda i, j, kk: (kk, i + a_off))
        b_specs = [pl.BlockSpec((tk, tn), lambda i, j, kk: (kk, j))]
    assert m % tm == 0 and n % tn == 0 and k % tk == 0, (name, m, n, k, tm, tn, tk)
    nk = k // tk
    nb = len(bs)

    def body(a_ref, *refs):
        o_ref = refs[nb]
        if pair:
            bv = jnp.where(pl.program_id(axis) < half, refs[0][...], refs[1][...])
        else:
            bv = refs[0][...]
        part = _dot(a_ref[...], bv, dims)
        if nk == 1:
            o_ref[...] = part.astype(o_ref.dtype)
            return
        acc_ref = refs[nb + 1]
        kk = pl.program_id(2)

        @pl.when(kk == 0)
        def _():
            acc_ref[...] = part

        @pl.when(kk > 0)
        def _():
            acc_ref[...] += part

        @pl.when(kk == nk - 1)
        def _():
            o_ref[...] = acc_ref[...].astype(o_ref.dtype)

    return pl.pallas_call(
        body,
        name=name,
        grid=(m // tm, n // tn, nk),
        in_specs=[a_spec] + b_specs,
        out_specs=pl.BlockSpec((tm, tn), lambda i, j, kk: (i, j)),
        out_shape=jax.ShapeDtypeStruct((m, n), out_dtype),
        scratch_shapes=[] if nk == 1 else [pltpu.VMEM((tm, tn), F32)],
        compiler_params=_cp("parallel", "parallel", "arbitrary"),
    )(a, *bs)


def _row(c):
    return pl.BlockSpec((TR, c), lambda i: (i, 0))


def _rowcol(width, cb):
    return pl.BlockSpec((TR, width), lambda i: (i, cb))


def _full(shape):
    return pl.BlockSpec(shape, lambda i: (0,) * len(shape))


def _mod_row(mc_ref, mx_ref, k, is_ctx):
    return jnp.where(is_ctx, mc_ref[k:k + 1, :], mx_ref[k:k + 1, :])


def _z_specs():
    return [pl.BlockSpec((TR, D), lambda i: (jnp.minimum(i, NCT - 1), 0)),
            pl.BlockSpec((TR, D), lambda i: (jnp.maximum(i - NCT, 0), 0))]


def _z_tile(c_ref, x_ref, is_ctx):
    return jnp.where(is_ctx, c_ref[...], x_ref[...])


def _acc_row(ref, k, val):
    ref[k:k + 1, :] += val


def _acc_mod(ref, k, is_ctx, val):
    zero = jnp.zeros_like(val)
    ref[k:k + 1, :] += jnp.where(is_ctx, val, zero)
    ref[k + 1:k + 2, :] += jnp.where(is_ctx, zero, val)


def _prenorm(z, nw, modc, modx, i_shift, i_scale, name):
    t = z[0].shape[0] + z[1].shape[0]

    def body(zc_ref, zx_ref, nw_ref, mc_ref, mx_ref, h_ref):
        is_ctx = pl.program_id(0) < NCT
        x = _z_tile(zc_ref, zx_ref, is_ctx)
        n = x * _rstd(x) * nw_ref[...]
        h = n * (1.0 + _mod_row(mc_ref, mx_ref, i_scale, is_ctx)) + _mod_row(mc_ref, mx_ref, i_shift, is_ctx)
        h_ref[...] = h.astype(BF16)

    return pl.pallas_call(
        body, name=name, grid=(t // TR,),
        in_specs=_z_specs() + [_full((1, D)), _full((8, D)), _full((8, D))],
        out_specs=_row(D),
        out_shape=jax.ShapeDtypeStruct((t, D), BF16),
        compiler_params=_cp("parallel"),
    )(*z, nw, modc, modx)


def _hg_lb(lb_ref, d):
    a0 = lb_ref[0, d:d + 1, :]
    a1 = lb_ref[1, d:d + 1, :]
    mx = jnp.maximum(a0, a1)
    e0 = jnp.exp(a0 - mx)
    e1 = jnp.exp(a1 - mx)
    return e0 / (e0 + e1)


def _log_sigmoid(x):
    return jnp.minimum(x, 0.0) - jnp.log(1.0 + jnp.exp(-jnp.abs(x)))


def _gates_fwd(p, hg_lb, wgk, bgk):
    t = p.shape[0]
    seg = lambda j: _rowcol(HW, MAIN0 // HW + j)

    def body(hq_ref, hi_ref, hf_ref, hb_ref, gq_ref, gk_ref, gv_ref, lr_ref, lb_ref, wgk_ref, bgk_ref,
             q_ref, v_ref, kf_ref, kb_ref, gf_ref, gb_ref):
        q_ref[:, :HW] = _silu(hq_ref[...].astype(F32)).astype(BF16)
        q_ref[:, HW:] = (gq_ref[...].astype(F32) * (DH ** -0.5)).astype(BF16)
        v_ref[:, :HW] = hi_ref[...]
        v_ref[:, HW:] = gv_ref[...]
        xg = _dot(lr_ref[...].astype(BF16), wgk_ref[...], NN) + bgk_ref[...]
        for d, (raw_ref, k_ref, g_ref) in enumerate(((hf_ref, kf_ref, gf_ref), (hb_ref, kb_ref, gb_ref))):
            lbd = _hg_lb(lb_ref, d)
            f = lbd + (1.0 - lbd) * _sig(raw_ref[...].astype(F32))
            k_ref[:, :HW] = (1.0 - f).astype(BF16)
            k_ref[:, HW:] = gk_ref[...]
            g_ref[:, :HW] = jnp.log(f)
            g_ref[:, HW:] = _log_sigmoid(xg[:, d * HW:(d + 1) * HW]) * (1.0 / GLA_NORM)

    out = jax.ShapeDtypeStruct((t, D), F32)
    outb = jax.ShapeDtypeStruct((t, D), BF16)
    return pl.pallas_call(
        body, name="gates_fwd", grid=(t // TR,),
        in_specs=[seg(0), seg(1), seg(2), seg(3), seg(5), seg(6), seg(7), _rowcol(DH, LR0 // DH),
                  _full((2, 2, HW)), _full((DH, D)), _full((1, D))],
        out_specs=[_row(D)] * 6,
        out_shape=[outb] * 4 + [out] * 2,
        compiler_params=_cp("parallel"),
    )(p, p, p, p, p, p, p, p, hg_lb, wgk, bgk)


def _post_fwd(o_fw, o_bw, p, onw):
    t = o_fw.shape[0]

    def body(of_ref, ob_ref, g1_ref, g2_ref, w_ref, y_ref):
        for h in range(NH):
            sl = slice(h * DH, (h + 1) * DH)
            o = of_ref[:, sl] + ob_ref[:, sl]
            g_ref = g1_ref if h < NH // 2 else g2_ref
            gs = slice((h % (NH // 2)) * DH, (h % (NH // 2) + 1) * DH)
            n = o * _rstd(o) * w_ref[:, sl]
            y_ref[:, sl] = (n * _silu(g_ref[:, gs].astype(F32))).astype(BF16)

    return pl.pallas_call(
        body, name="post_fwd", grid=(t // TR,),
        in_specs=[_row(D), _row(D), _rowcol(HW, MAIN0 // HW + 4), _rowcol(HW, MAIN0 // HW + 8), _full((1, D))],
        out_specs=_row(D),
        out_shape=jax.ShapeDtypeStruct((t, D), BF16),
        compiler_params=_cp("parallel"),
    )(o_fw, o_bw, p, p, onw)


def _gate_window_specs(col0):
    return [_rowcol(HW, col0 // HW), _rowcol(HW, col0 // HW + 1), _rowcol(DH, (col0 + 2 * HW) // DH)]


def _gate_window(refs):
    return jnp.concatenate([r[...].astype(F32) for r in refs], axis=1)


def _merge_fwd(p, u1, u2):
    t = p.shape[0]

    def body(a0, a1, a2, b0, b1, b2, u1_ref, u2_ref, m_ref):
        f = lambda r: r[...].astype(F32)
        m_ref[...] = (_sig(_gate_window((a0, a1, a2))) * f(u1_ref)
                      + _sig(_gate_window((b0, b1, b2))) * f(u2_ref)).astype(BF16)

    return pl.pallas_call(
        body, name="merge_fwd", grid=(t // TR,),
        in_specs=_gate_window_specs(GATE_HG0) + _gate_window_specs(GATE_GLA0) + [_row(GW), _row(GW)],
        out_specs=_row(GW),
        out_shape=jax.ShapeDtypeStruct((t, GW), BF16),
        compiler_params=_cp("parallel"),
    )(p, p, p, p, p, p, u1, u2)


def _mid_fwd(z, y1, nw_post, nw_pre, modc, modx):
    t = y1.shape[0]

    def body(zc_ref, zx_ref, y_ref, wpo_ref, wpr_ref, mc_ref, mx_ref, z1_ref, h_ref):
        is_ctx = pl.program_id(0) < NCT
        y = y_ref[...].astype(F32)
        z1 = _z_tile(zc_ref, zx_ref, is_ctx) + _mod_row(mc_ref, mx_ref, 2, is_ctx) * (y * _rstd(y) * wpo_ref[...])
        z1_ref[...] = z1
        n = z1 * _rstd(z1) * wpr_ref[...]
        h = n * (1.0 + _mod_row(mc_ref, mx_ref, 4, is_ctx)) + _mod_row(mc_ref, mx_ref, 3, is_ctx)
        h_ref[...] = h.astype(BF16)

    return pl.pallas_call(
        body, name="mid_fwd", grid=(t // TR,),
        in_specs=_z_specs() + [_row(D), _full((1, D)), _full((1, D)), _full((8, D)), _full((8, D))],
        out_specs=[_row(D), _row(D)],
        out_shape=[jax.ShapeDtypeStruct((t, D), F32), jax.ShapeDtypeStruct((t, D), BF16)],
        compiler_params=_cp("parallel"),
    )(*z, y1, nw_post, nw_pre, modc, modx)


def _swiglu_fwd(uv):
    t = uv.shape[0]

    def body(u_ref, v_ref, a_ref):
        a_ref[...] = (_silu(u_ref[...].astype(F32)) * v_ref[...].astype(F32)).astype(BF16)

    return pl.pallas_call(
        body, name="swiglu_fwd", grid=(t // TR,),
        in_specs=[_rowcol(D_FF, 0), _rowcol(D_FF, 1)],
        out_specs=_row(D_FF),
        out_shape=jax.ShapeDtypeStruct((t, D_FF), BF16),
        compiler_params=_cp("parallel"),
    )(uv, uv)


def _swiglu_bwd(uv, da):
    t = uv.shape[0]

    def body(u_ref, v_ref, da_ref, d_ref):
        u = u_ref[...].astype(F32)
        d = da_ref[...].astype(F32)
        d_ref[:, :D_FF] = (d * v_ref[...].astype(F32) * _dsilu(u)).astype(BF16)
        d_ref[:, D_FF:] = (d * _silu(u)).astype(BF16)

    return pl.pallas_call(
        body, name="swiglu_bwd", grid=(t // TR,),
        in_specs=[_rowcol(D_FF, 0), _rowcol(D_FF, 1), _row(D_FF)],
        out_specs=_row(2 * D_FF),
        out_shape=jax.ShapeDtypeStruct((t, 2 * D_FF), BF16),
        compiler_params=_cp("parallel"),
    )(uv, uv, da)


def _final(z1, y2, target, nw, modc, modx):
    t = z1.shape[0]

    def body(z1_ref, y_ref, tg_ref, w_ref, mc_ref, mx_ref, dz_ref, dy_ref, loss_ref, sm_ref):
        i = pl.program_id(0)
        is_ctx = i < NCT

        @pl.when(i == 0)
        def _():
            loss_ref[...] = jnp.zeros_like(loss_ref)
            sm_ref[...] = jnp.zeros_like(sm_ref)

        g = _mod_row(mc_ref, mx_ref, 5, is_ctx)
        y = y_ref[...].astype(F32)
        r = _rstd(y)
        w = w_ref[...]
        yr = y * r
        n = yr * w
        e = z1_ref[...] + g * n - tg_ref[...]
        lat = jnp.where(is_ctx, 0.0, 1.0)
        loss_ref[...] += lat * _colsum(e * e)
        dz = e * (lat / D)
        dz_ref[...] = dz
        _acc_mod(sm_ref, 0, is_ctx, _colsum(dz * n))
        dn = dz * g
        _acc_row(sm_ref, 2, _colsum(dn * yr))
        dy_ref[...] = _rms_bwd(dn * w, y, r).astype(BF16)

    return pl.pallas_call(
        body, name="final", grid=(t // TR,),
        in_specs=[_row(D), _row(D), pl.BlockSpec((TR, D), lambda i: (jnp.maximum(i - NCT, 0), 0)),
                  _full((1, D)), _full((8, D)), _full((8, D))],
        out_specs=[_row(D), _row(D), _full((1, D)), _full((8, D))],
        out_shape=[jax.ShapeDtypeStruct((t, D), F32), jax.ShapeDtypeStruct((t, D), BF16),
                   jax.ShapeDtypeStruct((1, D), F32), jax.ShapeDtypeStruct((8, D), F32)],
        compiler_params=_cp("arbitrary"),
    )(z1, y2, target, nw, modc, modx)


def _mid_bwd(dh2, dz, z1, y1, nw_post, nw_pre, modc, modx):
    t = z1.shape[0]

    def body(dh_ref, dz_ref, z1_ref, y_ref, wpo_ref, wpr_ref, mc_ref, mx_ref, dzo_ref, dy_ref, sm_ref):
        i = pl.program_id(0)
        is_ctx = i < NCT

        @pl.when(i == 0)
        def _():
            sm_ref[...] = jnp.zeros_like(sm_ref)

        dh = dh_ref[...].astype(F32)
        z1 = z1_ref[...]
        r = _rstd(z1)
        zr = z1 * r
        wpr = wpr_ref[...]
        n = zr * wpr
        _acc_mod(sm_ref, 0, is_ctx, _colsum(dh))
        _acc_mod(sm_ref, 2, is_ctx, _colsum(dh * n))
        dn = dh * (1.0 + _mod_row(mc_ref, mx_ref, 4, is_ctx))
        _acc_row(sm_ref, 6, _colsum(dn * zr))
        dz1 = dz_ref[...] + _rms_bwd(dn * wpr, z1, r)
        dzo_ref[...] = dz1
        y = y_ref[...].astype(F32)
        r1 = _rstd(y)
        yr = y * r1
        wpo = wpo_ref[...]
        g = _mod_row(mc_ref, mx_ref, 2, is_ctx)
        _acc_mod(sm_ref, 4, is_ctx, _colsum(dz1 * (yr * wpo)))
        dn1 = dz1 * g
        _acc_row(sm_ref, 7, _colsum(dn1 * yr))
        dy_ref[...] = _rms_bwd(dn1 * wpo, y, r1).astype(BF16)

    return pl.pallas_call(
        body, name="mid_bwd", grid=(t // TR,),
        in_specs=[_row(D)] * 4 + [_full((1, D)), _full((1, D)), _full((8, D)), _full((8, D))],
        out_specs=[_row(D), _row(D), _full((8, D))],
        out_shape=[jax.ShapeDtypeStruct((t, D), F32), jax.ShapeDtypeStruct((t, D), BF16),
                   jax.ShapeDtypeStruct((8, D), F32)],
        compiler_params=_cp("arbitrary"),
    )(dh2, dz, z1, y1, nw_post, nw_pre, modc, modx)


def _pre_bwd(dh1, dz, z, nw, modc, modx):
    t = dh1.shape[0]

    def body(dh_ref, dz_ref, zc_ref, zx_ref, w_ref, mc_ref, mx_ref, dzo_ref, sm_ref):
        i = pl.program_id(0)
        is_ctx = i < NCT

        @pl.when(i == 0)
        def _():
            sm_ref[...] = jnp.zeros_like(sm_ref)

        dh = dh_ref[...].astype(F32)
        x = _z_tile(zc_ref, zx_ref, is_ctx)
        r = _rstd(x)
        xr = x * r
        w = w_ref[...]
        _acc_mod(sm_ref, 0, is_ctx, _colsum(dh))
        _acc_mod(sm_ref, 2, is_ctx, _colsum(dh * (xr * w)))
        dn = dh * (1.0 + _mod_row(mc_ref, mx_ref, 1, is_ctx))
        _acc_row(sm_ref, 4, _colsum(dn * xr))
        dzo_ref[...] = dz_ref[...] + _rms_bwd(dn * w, x, r)

    return pl.pallas_call(
        body, name="pre_bwd", grid=(t // TR,),
        in_specs=[_row(D)] * 2 + _z_specs() + [_full((1, D)), _full((8, D)), _full((8, D))],
        out_specs=[pl.BlockSpec((TR, D), lambda i: (jnp.maximum(i - NCT, 0), 0)), _full((8, D))],
        out_shape=[jax.ShapeDtypeStruct((t - CTX, D), F32), jax.ShapeDtypeStruct((8, D), F32)],
        compiler_params=_cp("arbitrary"),
    )(dh1, dz, *z, nw, modc, modx)


def _merge_bwd(dm, p, u1, u2):
    t = dm.shape[0]

    def body(dm_ref, a0, a1, a2, b0, b1, b2, u1_ref, u2_ref, du1_ref, du2_ref, dg_ref):
        dm_ = dm_ref[...].astype(F32)
        s1 = _sig(_gate_window((a0, a1, a2)))
        s2 = _sig(_gate_window((b0, b1, b2)))
        du1_ref[...] = (dm_ * s1).astype(BF16)
        du2_ref[...] = (dm_ * s2).astype(BF16)
        dg_ref[:, :GW] = (dm_ * u1_ref[...].astype(F32) * s1 * (1.0 - s1)).astype(BF16)
        dg_ref[:, GW:] = (dm_ * u2_ref[...].astype(F32) * s2 * (1.0 - s2)).astype(BF16)

    return pl.pallas_call(
        body, name="merge_bwd", grid=(t // TR,),
        in_specs=[_row(GW)] + _gate_window_specs(GATE_HG0) + _gate_window_specs(GATE_GLA0) + [_row(GW), _row(GW)],
        out_specs=[_row(GW), _row(GW), _row(2 * GW)],
        out_shape=[jax.ShapeDtypeStruct((t, GW), BF16), jax.ShapeDtypeStruct((t, GW), BF16),
                   jax.ShapeDtypeStruct((t, 2 * GW), BF16)],
        compiler_params=_cp("parallel"),
    )(dm, p, p, p, p, p, p, u1, u2)


def _post_bwd(dy_hg, dy_gla, o_fw, o_bw, p, onw):
    t = o_fw.shape[0]

    def body(d1_ref, d2_ref, of_ref, ob_ref, g1_ref, g2_ref, w_ref, do_ref, dg_ref, sm_ref):
        @pl.when(pl.program_id(0) == 0)
        def _():
            sm_ref[...] = jnp.zeros_like(sm_ref)

        for h in range(NH):
            sl = slice(h * DH, (h + 1) * DH)
            gs = slice((h % (NH // 2)) * DH, (h % (NH // 2) + 1) * DH)
            g_ref, d_ref = (g1_ref, d1_ref) if h < NH // 2 else (g2_ref, d2_ref)
            o = of_ref[:, sl] + ob_ref[:, sl]
            r = _rstd(o)
            orr = o * r
            w = w_ref[:, sl]
            gt = g_ref[:, gs].astype(F32)
            dy = d_ref[:, gs].astype(F32)
            dg_ref[:, sl] = (dy * (orr * w) * _dsilu(gt)).astype(BF16)
            dn = dy * _silu(gt)
            sm_ref[0:1, sl] += _colsum(dn * orr)
            do_ref[:, sl] = _rms_bwd(dn * w, o, r)

    return pl.pallas_call(
        body, name="post_bwd", grid=(t // TR,),
        in_specs=[_row(HW), _row(HW), _row(D), _row(D), _rowcol(HW, MAIN0 // HW + 4), _rowcol(HW, MAIN0 // HW + 8),
                  _full((1, D))],
        out_specs=[_row(D), _row(D), _full((8, D))],
        out_shape=[jax.ShapeDtypeStruct((t, D), F32), jax.ShapeDtypeStruct((t, D), BF16),
                   jax.ShapeDtypeStruct((8, D), F32)],
        compiler_params=_cp("arbitrary"),
    )(dy_hg, dy_gla, o_fw, o_bw, p, p, onw)


def _gates_bwd(p, hg_lb, wgk, bgk, dgm, dgo, dq_f, dq_b, dv_f, dv_b, dk_f, dk_b, dg_f, dg_b):
    t = p.shape[0]
    seg = lambda j: _rowcol(HW, MAIN0 // HW + j)

    def body(hq_ref, hf_ref, hb_ref, lr_ref, lb_ref, wgk_ref, bgk_ref, dgm_ref, dgo_ref,
             dqf_ref, dqb_ref, dvf_ref, dvb_ref, dkf_ref, dkb_ref, dgf_ref, dgb_ref,
             dp_ref, dlb_ref, dw_ref, db_ref):
        @pl.when(pl.program_id(0) == 0)
        def _():
            dlb_ref[...] = jnp.zeros_like(dlb_ref)
            dw_ref[...] = jnp.zeros_like(dw_ref)
            db_ref[...] = jnp.zeros_like(db_ref)

        c0 = MAIN0

        def put(j, val):
            dp_ref[:, c0 + j * HW:c0 + (j + 1) * HW] = val.astype(BF16)

        dq = dqf_ref[...].astype(F32) + dqb_ref[...].astype(F32)
        dv = dvf_ref[...].astype(F32) + dvb_ref[...].astype(F32)
        put(0, dq[:, :HW] * _dsilu(hq_ref[...].astype(F32)))
        put(1, dv[:, :HW])
        put(5, dq[:, HW:] * (DH ** -0.5))
        put(7, dv[:, HW:])
        put(6, dkf_ref[:, HW:].astype(F32) + dkb_ref[:, HW:].astype(F32))
        dp_ref[:, c0 + 4 * HW:c0 + 5 * HW] = dgo_ref[:, :HW]
        dp_ref[:, c0 + 8 * HW:c0 + 9 * HW] = dgo_ref[:, HW:]
        lr = lr_ref[...].astype(BF16)
        xg = _dot(lr, wgk_ref[...], NN) + bgk_ref[...]
        dxg = []
        for d, (raw_ref, dk_ref, dg_ref) in enumerate(((hf_ref, dkf_ref, dgf_ref), (hb_ref, dkb_ref, dgb_ref))):
            lbd = _hg_lb(lb_ref, d)
            s = _sig(raw_ref[...].astype(F32))
            f = lbd + (1.0 - lbd) * s
            df = dg_ref[:, :HW] / f - dk_ref[:, :HW].astype(F32)
            put(2 + d, df * (1.0 - lbd) * s * (1.0 - s))
            dlb_ref[d:d + 1, :] += _colsum(df * (1.0 - s)) * (lbd * (1.0 - lbd))
            dxg.append(dg_ref[:, HW:] * (1.0 / GLA_NORM) * _sig(-xg[:, d * HW:(d + 1) * HW]))
        dxg = jnp.concatenate(dxg, axis=1)
        db_ref[0:1, :] += _colsum(dxg)
        dxg_b = dxg.astype(BF16)
        dw_ref[...] += _dot(lr, dxg_b, TN)
        dlr = _dot(dxg_b, wgk_ref[...], NT)
        dp_ref[:, LR0:LR0 + DH] = (dlr + dgm_ref[:, :DH].astype(F32)).astype(BF16)
        dp_ref[:, LR0 + DH:GATE_GLA0] = dgm_ref[:, DH:D]
        dp_ref[:, GATE_GLA0:GATE_GLA0 + DH] = dgm_ref[:, D:GW] + dgm_ref[:, GW:GW + DH]
        dp_ref[:, GATE_GLA0 + DH:GATE_GLA0 + GW] = dgm_ref[:, GW + DH:]
        dp_ref[:, GATE_GLA0 + GW:] = jnp.zeros((TR, W_IN_COLS - GATE_GLA0 - GW), BF16)

    return pl.pallas_call(
        body, name="gates_bwd", grid=(t // TR,),
        in_specs=[seg(0), seg(2), seg(3), _rowcol(DH, LR0 // DH), _full((2, 2, HW)), _full((DH, D)), _full((1, D)),
                  _row(2 * GW), _row(D)] + [_row(D)] * 8,
        out_specs=[_row(W_IN_COLS), _full((8, HW)), _full((DH, D)), _full((8, D))],
        out_shape=[jax.ShapeDtypeStruct((t, W_IN_COLS), BF16), jax.ShapeDtypeStruct((8, HW), F32),
                   jax.ShapeDtypeStruct((DH, D), F32), jax.ShapeDtypeStruct((8, D), F32)],
        compiler_params=_cp("arbitrary"),
    )(p, p, p, p, hg_lb, wgk, bgk, dgm, dgo, dq_f, dq_b, dv_f, dv_b, dk_f, dk_b, dg_f, dg_b)


def _scan_consts(rev):
    r = lax.broadcasted_iota(jnp.int32, (CHUNK, CHUNK), 0)
    u = lax.broadcasted_iota(jnp.int32, (CHUNK, CHUNK), 1)
    rp = lax.broadcasted_iota(jnp.int32, (CHUNK, 1), 0)
    if rev:
        r, u, rp = CHUNK - 1 - r, CHUNK - 1 - u, CHUNK - 1 - rp
    tri = jnp.where(u <= r, 1.0, 0.0).astype(F32)
    tri_t = jnp.where(r <= u, 1.0, 0.0).astype(F32)
    lv = []
    for b in LEVELS:
        sh = b.bit_length() - 1
        pair = ((r >> sh) == (u >> sh) + 1) & (((u >> sh) & 1) == 0)
        pair_t = ((u >> sh) == (r >> sh) + 1) & (((r >> sh) & 1) == 0)
        tside = ((rp >> sh) & 1) == 1
        lv.append((pair, pair_t, tside, jnp.where(tside, 1.0, -1.0).astype(F32)))
    bd = LEVELS[-1].bit_length() - 1
    diag = ((r >> bd) == (u >> bd)) & (u <= r)
    diag_t = ((r >> bd) == (u >> bd)) & (r <= u)
    return tri, tri_t, lv, diag, diag_t


def _row_of(pos, rev):
    return CHUNK - 1 - pos if rev else pos


def _chunk_terms(cum, b_scr, consts, rev):
    _, _, lv, _, _ = consts
    terms = []
    for b, (_, _, _, sgn) in zip(LEVELS, lv):
        pieces = []
        for j in range(CHUNK // (2 * b)):
            row = _row_of(2 * b * j + b - 1, rev)
            pieces.append(jnp.broadcast_to(b_scr[row:row + 1, :], (2 * b, DH)))
        if rev:
            pieces = pieces[::-1]
        bnd = pieces[0] if len(pieces) == 1 else jnp.concatenate(pieces, axis=0)
        terms.append(jnp.exp((cum - bnd) * sgn))
    b = LEVELS[-1]
    pieces = []
    for j in range(CHUNK // b):
        if j == 0:
            pieces.append(jnp.zeros((b, DH), F32))
        else:
            row = _row_of(b * j - 1, rev)
            pieces.append(jnp.broadcast_to(b_scr[row:row + 1, :], (b, DH)))
    if rev:
        pieces = pieces[::-1]
    start = jnp.concatenate(pieces, axis=0)
    wq = jnp.exp(jnp.minimum(cum - start, 0.0))
    wk = jnp.exp(jnp.minimum(start - cum, EXP_CLAMP))
    terms.append((wq, wk))
    return terms


def _run_staged(units):
    live = list(units)
    while live:
        nxt = []
        for u in live:
            try:
                next(u)
                nxt.append(u)
            except StopIteration:
                pass
        live = nxt


SCAN_TB = 256
SCAN_CB = SCAN_TB // CHUNK


def _block_order(i, ntb, rev):
    nctx = CTX // SCAN_TB
    if not rev:
        return i
    return jnp.where(i < nctx, nctx - 1 - i, ntb - 1 - (i - nctx))


def _chunk_in_block(j, rev):
    return SCAN_CB - 1 - j if rev else j


def _scan_fwd(q, k, v, g, rev):
    t = q.shape[0]
    nc = t // CHUNK
    hpb = SCAN_HEADS_FWD

    def body(q_ref, k_ref, v_ref, g_ref, o_ref, st_ref, s_scr, b_scr):
        consts = _scan_consts(rev)
        _, _, lv, diag, _ = consts
        masks = [lvl[0] for lvl in lv] + [diag]

        @pl.when(pl.program_id(1) == 0)
        def _():
            s_scr[...] = jnp.zeros_like(s_scr)

        tri = consts[0]
        state = {hh: s_scr[hh] for hh in range(hpb)}

        def unit(hh, j):
            sl = slice(hh * DH, (hh + 1) * DH)
            c = _chunk_in_block(j, rev)
            rows = slice(c * CHUNK, (c + 1) * CHUNK)
            b_ref = b_scr.at[hh * SCAN_CB + j]
            qc, kc, vc, gc = q_ref[rows, sl], k_ref[rows, sl], v_ref[rows, sl], g_ref[rows, sl]
            cum = _split_dot(tri, gc)
            b_ref[...] = cum
            yield
            terms = _chunk_terms(cum, b_ref, consts, rev)
            qf, kf = qc.astype(F32), kc.astype(F32)
            xs = [(jnp.where(tside, qf, kf) * w).astype(BF16) for w, (_, _, tside, _) in zip(terms[:-1], lv)]
            qd, kd = (qf * terms[-1][0]).astype(BF16), (kf * terms[-1][1]).astype(BF16)
            tot = _colsum(gc)
            qe = (qf * jnp.exp(cum)).astype(BF16)
            ke = (kf * jnp.exp(tot - cum)).astype(BF16)
            vb = vc.astype(BF16)
            yield
            scs = [_dot(x, x, NT) for x in xs] + [_dot(qd, kd, NT)]
            kv = _dot(vb, ke, TN)
            yield
            a = jnp.zeros((CHUNK, CHUNK), F32)
            for sc, m in zip(scs, masks):
                a = a + jnp.where(m, sc, 0.0)
            o_intra = _dot(a.astype(BF16), vb, NN)
            yield
            st = state[hh]
            st_ref[hh, c] = st
            o_ref[rows, sl] = o_intra + _dot(qe, st.astype(BF16), NT)
            state[hh] = st * jnp.exp(tot) + kv
            yield

        _run_staged([unit(hh, j) for hh in range(hpb) for j in range(SCAN_CB)])
        for hh in range(hpb):
            s_scr[hh] = state[hh]

    ntb = t // SCAN_TB
    col = pl.BlockSpec((SCAN_TB, hpb * DH), lambda h, i: (_block_order(i, ntb, rev), h))
    return pl.pallas_call(
        body, name="scan_fwd_" + ("bw" if rev else "fw"), grid=(NH // hpb, ntb),
        in_specs=[col] * 4,
        out_specs=[col, pl.BlockSpec((hpb, SCAN_CB, DH, DH), lambda h, i: (h, _block_order(i, ntb, rev), 0, 0))],
        out_shape=[jax.ShapeDtypeStruct((t, D), F32), jax.ShapeDtypeStruct((NH, nc, DH, DH), F32)],
        scratch_shapes=[pltpu.VMEM((hpb, DH, DH), F32), pltpu.VMEM((hpb * SCAN_CB, CHUNK, DH), F32)],
        compiler_params=_cp("parallel", "arbitrary"),
    )(q, k, v, g)


def _scan_bwd(q, k, v, g, do, states, rev):
    t = q.shape[0]
    nc = t // CHUNK
    hpb = SCAN_HEADS_BWD

    def body(q_ref, k_ref, v_ref, g_ref, do_ref, st_ref, dq_ref, dk_ref, dv_ref, dg_ref, ds_scr, b_scr):
        consts = _scan_consts(rev)
        _, tri_t, lv, diag, diag_t = consts
        masks = [(lvl[0], lvl[1]) for lvl in lv] + [(diag, diag_t)]
        @pl.when(pl.program_id(1) == 0)
        def _():
            ds_scr[...] = jnp.zeros_like(ds_scr)

        tri = consts[0]
        dstate = {hh: ds_scr[hh] for hh in range(hpb)}

        def unit(hh, jj):
            sl = slice(hh * DH, (hh + 1) * DH)
            c = _chunk_in_block(SCAN_CB - 1 - jj, rev)
            rows = slice(c * CHUNK, (c + 1) * CHUNK)
            b_ref = b_scr.at[hh * SCAN_CB + jj]
            qc, kc, vc, gc = q_ref[rows, sl], k_ref[rows, sl], v_ref[rows, sl], g_ref[rows, sl]
            dob = do_ref[rows, sl].astype(BF16)
            vb = vc.astype(BF16)
            cum = _split_dot(tri, gc)
            b_ref[...] = cum
            da = _dot(dob, vb, NT)
            da_t = _dot(vb, dob, NT)
            yield
            terms = _chunk_terms(cum, b_ref, consts, rev)
            qf, kf = qc.astype(F32), kc.astype(F32)
            xs = [(jnp.where(tside, qf, kf) * w).astype(BF16) for w, (_, _, tside, _) in zip(terms[:-1], lv)]
            wqd, wkd = terms[-1]
            qdb, kdb = (qf * wqd).astype(BF16), (kf * wkd).astype(BF16)
            tot = _colsum(gc)
            e_tot = jnp.exp(tot)
            e_b = jnp.exp(cum)
            e_t = jnp.exp(tot - cum)
            qeb = (qf * e_b).astype(BF16)
            keb = (kf * e_t).astype(BF16)
            dsym = [(jnp.where(m, da, 0.0) + jnp.where(m_t, da_t, 0.0)).astype(BF16) for m, m_t in masks[:-1]]
            dad = (jnp.where(diag, da, 0.0).astype(BF16), jnp.where(diag_t, da_t, 0.0).astype(BF16))
            yield
            sym = [_dot(x, x, NT) for x in xs]
            dxs = [_dot(d, x, NN) for d, x in zip(dsym, xs)]
            at_d = _dot(kdb, qdb, NT)
            dqt_d = _dot(dad[0], kdb, NN)
            dkt_d = _dot(dad[1], qdb, NN)
            qd = _dot(dob, qeb, TN)
            yield
            a_t = jnp.where(diag_t, at_d, 0.0)
            dq = dqt_d * wqd
            dk = dkt_d * wkd
            db = dqt_d * qdb.astype(F32) - dkt_d * kdb.astype(F32)
            for s, dx, x, w, (_, m_t, tside, sgn) in zip(sym, dxs, xs, terms[:-1], lv):
                a_t = a_t + jnp.where(m_t, s, 0.0)
                dxw = dx * w
                dq = dq + jnp.where(tside, dxw, 0.0)
                dk = dk + jnp.where(tside, 0.0, dxw)
                db = db + (dx * x.astype(F32)) * sgn
            dv_intra = _dot(a_t.astype(BF16), dob, NN)
            st = st_ref[hh, c]
            stb = st.astype(BF16)
            dqe = _dot(dob, stb, NN)
            yield
            dst = dstate[hh]
            dstb = dst.astype(BF16)
            dstate[hh] = dst * e_tot + qd
            dv_ref[rows, sl] = (dv_intra + _dot(keb, dstb, NT)).astype(BF16)
            dke = _dot(vb, dstb, NN)
            yield
            qe = qeb.astype(F32)
            ke = keb.astype(F32)
            dq_ref[rows, sl] = (dq + dqe * e_b).astype(BF16)
            dk_ref[rows, sl] = (dk + dke * e_t).astype(BF16)
            db = db + dqe * qe - dke * ke
            dtot = _colsum(dstb.astype(F32) * stb.astype(F32)) * e_tot + _colsum(dke * ke)
            dg_ref[rows, sl] = _split_dot(tri_t, db) + dtot
            yield

        _run_staged([unit(hh, jj) for hh in range(hpb) for jj in range(SCAN_CB)])
        for hh in range(hpb):
            ds_scr[hh] = dstate[hh]

    ntb = t // SCAN_TB
    blk = lambda i: _block_order(ntb - 1 - i, ntb, rev)
    col = pl.BlockSpec((SCAN_TB, hpb * DH), lambda h, i: (blk(i), h))
    out = jax.ShapeDtypeStruct((t, D), F32)
    outb = jax.ShapeDtypeStruct((t, D), BF16)
    return pl.pallas_call(
        body, name="scan_bwd_" + ("bw" if rev else "fw"), grid=(NH // hpb, ntb),
        in_specs=[col] * 5 + [pl.BlockSpec((hpb, SCAN_CB, DH, DH), lambda h, i: (h, blk(i), 0, 0))],
        out_specs=[col] * 4,
        out_shape=[outb] * 3 + [out],
        scratch_shapes=[pltpu.VMEM((hpb, DH, DH), F32), pltpu.VMEM((hpb * SCAN_CB, CHUNK, DH), F32)],
        compiler_params=_cp("parallel", "arbitrary"),
    )(q, k, v, g, do, states)


W_IN_GRAD_CHUNKS = (("a", (0, 512)), ("b", (0, 128)), ("b", (128, 512)))
W_IN_REF = 6688


def _layout_w_in(w):
    return jnp.pad(w, ((0, 0), (0, W_IN_COLS - W_IN_REF)))


def _unlayout_w_in(d):
    return d[:, :W_IN_REF]


W_IN_PAD = 896


def _assemble_w_in(g):
    n, r, wp = g.shape
    tr = 256
    tiles = wp // DH

    def body(g_ref, o_ref):
        lane = lax.broadcasted_iota(jnp.int32, (tr, DH), 1)
        for t in range(W_IN_COLS // DH):
            acc = None
            for j in range(n):
                c = DH * t - W_IN_SHARD * j
                if c <= -DH or c >= W_IN_SHARD:
                    continue
                k, s = divmod(c, DH)
                lo = g_ref[j, :, k * DH:(k + 1) * DH] if 0 <= k < tiles else None
                hi = g_ref[j, :, (k + 1) * DH:(k + 2) * DH] if s and 0 <= k + 1 < tiles else None
                if s:
                    zero = jnp.zeros((tr, DH), g.dtype)
                    lo = zero if lo is None else pltpu.roll(lo, DH - s, 1)
                    hi = zero if hi is None else pltpu.roll(hi, DH - s, 1)
                    part = jnp.where(lane < DH - s, lo, hi)
                else:
                    part = lo
                acc = part if acc is None else acc + part
            o_ref[:, t * DH:(t + 1) * DH] = jnp.zeros((tr, DH), g.dtype) if acc is None else acc

    return pl.pallas_call(
        body, name="assemble_w_in", grid=(r // tr,),
        in_specs=[pl.BlockSpec((n, tr, wp), lambda i: (0, i, 0))],
        out_specs=pl.BlockSpec((tr, W_IN_COLS), lambda i: (i, 0)),
        out_shape=jax.ShapeDtypeStruct((r, W_IN_COLS), g.dtype),
        compiler_params=_cp("parallel"),
    )(g)


def _gate_cols(w):
    return jnp.pad(w, ((0, 0), (GOFF, GW - GOFF - D)))


def _gate_rows(w):
    return jnp.pad(w, ((GOFF, GW - GOFF - D), (0, 0)))


def _layout_wgk(w):
    r = w.shape[1]
    top = jnp.concatenate([w[0], jnp.zeros_like(w[0])], axis=1)
    bot = jnp.concatenate([jnp.zeros_like(w[1]), w[1]], axis=1)
    return jnp.concatenate([top, bot, jnp.zeros((DH - 2 * r, D), w.dtype)], axis=0)


def _unlayout_wgk(d, r=16):
    return jnp.stack([d[:r, :HW], d[r:2 * r, HW:]])


def _local_step(z, target, modc, modx, norms, onw, hg_lb, wgk, bgk, get_w_in, get_mix, get_ffn, send):
    n_pre1, n_post1, n_pre2, n_post2 = norms
    t = z[0].shape[0] + z[1].shape[0]
    tm = 1152 if t % 1152 == 0 else 256
    h1 = _prenorm(z, n_pre1, modc, modx, 0, 1, "prenorm1")
    w_in = get_w_in(h1)
    p = _matmul(h1, w_in, NN, BF16, "mm_in", t, 1024, D)
    q, v, k_f, k_b, g_f, g_b = _gates_fwd(p, hg_lb, wgk, bgk)
    o_f, st_f = _scan_fwd(q, k_f, v, g_f, False)
    o_b, st_b = _scan_fwd(q, k_b, v, g_b, True)
    y = _post_fwd(o_f, o_b, p, onw)
    w_br_hg, w_br_gla, w_out = get_mix(y)
    u1 = _matmul(y, w_br_hg, NN, BF16, "mm_br_hg", tm, GW, HW, a_off=0)
    u2 = _matmul(y, w_br_gla, NN, BF16, "mm_br_gla", tm, GW, HW, a_off=1)
    merged = _merge_fwd(p, u1, u2)
    y1 = _matmul(merged, w_out, NN, BF16, "mm_out", tm, 512, GW)
    z1, h2 = _mid_fwd(z, y1, n_post1, n_pre2, modc, modx)
    w_gu_t, w_down = get_ffn(h2)
    uv = _matmul(h2, w_gu_t, NT, BF16, "mm_gu", t, D_FF // 2, D)
    act = _swiglu_fwd(uv)
    y2 = _matmul(act, w_down, NN, BF16, "mm_down", t, 512, D_FF)
    dz, dy2, loss_vec, sm_final = _final(z1, y2, target, n_post2, modc, modx)
    dact = _matmul(dy2, w_down, NT, BF16, "mm_down_dx", t, D_FF // 2, D)
    d_w_down = _matmul(act, dy2, TN, BF16, "mm_down_dw", D_FF // 2, 1024, t)
    duv = _swiglu_bwd(uv, dact)
    dh2 = _matmul(duv, w_gu_t, NN, BF16, "mm_gu_dx", tm, 512, D_FF)
    d_w_gate_t = _matmul(duv, h2, TN, BF16, "mm_gate_dw", D_FF // 2, 1024, t, a_off=0, m_out=D_FF)
    d_w_up_t = _matmul(duv, h2, TN, BF16, "mm_up_dw", D_FF // 2, 1024, t, a_off=2, m_out=D_FF)
    dh2 = send(("w_down", "w_gate_t", "w_up_t"), (d_w_down, d_w_gate_t, d_w_up_t), dh2)
    dz, dy1, sm_mid = _mid_bwd(dh2, dz, z1, y1, n_post1, n_pre2, modc, modx)
    dmerged = _matmul(dy1, w_out, NT, BF16, "mm_out_dx", tm, GW, D)
    d_w_out = _matmul(merged, dy1, TN, BF16, "mm_out_dw", GW, 512, t)
    du1, du2, dgm = _merge_bwd(dmerged, p, u1, u2)
    dy_hg = _matmul(du1, w_br_hg, NT, BF16, "mm_br_hg_dx", tm, HW, GW)
    dy_gla = _matmul(du2, w_br_gla, NT, BF16, "mm_br_gla_dx", tm, HW, GW)
    d_w_br_hg = _matmul(y, du1, TN, BF16, "mm_br_hg_dw", HW, GW, t, a_off=0, m_out=HW)
    d_w_br_gla = _matmul(y, du2, TN, BF16, "mm_br_gla_dw", HW, GW, t, a_off=1, m_out=HW)
    dy_hg = send(("w_out", "w_br_hg", "w_br_gla"), (d_w_out, d_w_br_hg, d_w_br_gla), dy_hg)
    do, dgo, sm_post = _post_bwd(dy_hg, dy_gla, o_f, o_b, p, onw)
    dq_f, dk_f, dv_f, dg_f = _scan_bwd(q, k_f, v, g_f, do, st_f, False)
    dq_b, dk_b, dv_b, dg_b = _scan_bwd(q, k_b, v, g_b, do, st_b, True)
    dp, d_lb, d_wgk, d_bgk = _gates_bwd(p, hg_lb, wgk, bgk, dgm, dgo, dq_f, dq_b, dv_f, dv_b, dk_f, dk_b, dg_f, dg_b)
    d_w_in_a = _matmul(h1, dp, TN, BF16, "mm_in_dw_a", 512, 1024, t, a_off=0, m_out=D // 2)
    dp = send(("w_in_a",), (d_w_in_a,), dp)
    d_w_in_b = _matmul(h1, dp, TN, BF16, "mm_in_dw_b", 512, 1024, t, a_off=1, m_out=D // 2)
    dp = send(("w_in_b",), (d_w_in_b,), dp)
    dh1 = _matmul(dp, w_in, NT, BF16, "mm_in_dx", tm, 512, W_IN_COLS // 2)
    grad_x, sm_pre = _pre_bwd(dh1, dz, z, n_pre1, modc, modx)
    return dict(loss_vec=loss_vec, grad_x=grad_x, sm_final=sm_final, sm_mid=sm_mid, sm_post=sm_post, sm_pre=sm_pre,
                d_lb=d_lb, d_wgk=d_wgk, d_bgk=d_bgk)


MESH = pl.DeviceIdType.MESH
ANY = pl.BlockSpec(memory_space=pl.ANY)
N_REL = N_DEV - 1


def _place():
    return lax.axis_index("x"), lax.axis_index("y"), lax.axis_index("c")


def _slot(p):
    return 4 * p[0] + 2 * p[1] + p[2]


def _all_gather(arrays, name):
    n = len(arrays)

    def body(*refs):
        ins, outs = refs[:n], refs[n:2 * n]
        send_sems, recv_sems, local_sems = refs[2 * n:]
        x, y, c = _place()
        me, sibling = (x, y, c), (x, y, 1 - c)
        chips = [(1 - x, y), (x, 1 - y), (1 - x, 1 - y)]

        def copy(a, k, block, to, src=None):
            dst = outs[a].at[_slot(block)]
            return pltpu.make_async_remote_copy(
                src_ref=dst if src is None else src, dst_ref=dst,
                send_sem=send_sems.at[N_REL * a + k], recv_sem=recv_sems.at[N_REL * a + k],
                device_id=to, device_id_type=MESH)

        mine = [pltpu.make_async_copy(ins[a], outs[a].at[_slot(me)], local_sems.at[a]) for a in range(n)]
        for cp in mine:
            cp.start()
        first = []
        for a in range(n):
            first.append(copy(a, 0, me, sibling, src=ins[a]))
            first += [copy(a, 1 + j, me, (*chip, c), src=ins[a]) for j, chip in enumerate(chips)]
        for cp in first:
            cp.start()
        passed = []
        for j, chip in enumerate(chips):
            for a in range(n):
                copy(a, 1 + j, (*chip, c), me).wait_recv()
                fwd = copy(a, 4 + j, (*chip, c), sibling)
                fwd.start()
                passed.append(fwd)
        for a in range(n):
            copy(a, 0, sibling, me).wait_recv()
        for j, chip in enumerate(chips):
            for a in range(n):
                copy(a, 4 + j, (*chip, 1 - c), me).wait_recv()
        for cp in first + passed:
            cp.wait_send()
        for cp in mine:
            cp.wait()

    return pl.pallas_call(
        body, name=name,
        in_specs=[ANY] * n, out_specs=[ANY] * n,
        out_shape=[jax.ShapeDtypeStruct((N_DEV,) + a.shape, a.dtype) for a in arrays],
        scratch_shapes=[pltpu.SemaphoreType.DMA((N_REL * n,)), pltpu.SemaphoreType.DMA((N_REL * n,)),
                        pltpu.SemaphoreType.DMA((n,))],
    )(*arrays)


def _exchange(arrays, name):
    n = len(arrays)

    def body(*refs):
        ins, outs = refs[:n], refs[n:2 * n]
        send_sems, recv_sems, local_sems = refs[2 * n:]
        x, y, c = _place()
        me = _slot((x, y, c))
        mine = [pltpu.make_async_copy(ins[a].at[me], outs[a].at[me], local_sems.at[a]) for a in range(n)]
        for cp in mine:
            cp.start()
        copies = []
        for a in range(n):
            for k in range(1, N_DEV):
                flip = lambda v, bit: 1 - v if bit else v
                peer = (flip(x, k & 4), flip(y, k & 2), flip(c, k & 1))
                copies.append(pltpu.make_async_remote_copy(
                    src_ref=ins[a].at[_slot(peer)], dst_ref=outs[a].at[me],
                    send_sem=send_sems.at[N_REL * a + k - 1], recv_sem=recv_sems.at[N_REL * a + k - 1],
                    device_id=peer, device_id_type=MESH))
                copies[-1].start()
        i = 0
        for a in range(n):
            for k in range(1, N_DEV):
                flip = lambda v, bit: 1 - v if bit else v
                peer = (flip(x, k & 4), flip(y, k & 2), flip(c, k & 1))
                pltpu.make_async_remote_copy(
                    src_ref=ins[a].at[_slot(peer)], dst_ref=outs[a].at[_slot(peer)],
                    send_sem=send_sems.at[N_REL * a + k - 1], recv_sem=recv_sems.at[N_REL * a + k - 1],
                    device_id=peer, device_id_type=MESH).wait_recv()
                i += 1
        for cp in copies:
            cp.wait_send()
        for cp in mine:
            cp.wait()

    return pl.pallas_call(
        body, name=name,
        in_specs=[ANY] * n, out_specs=[ANY] * n,
        out_shape=[jax.ShapeDtypeStruct(a.shape, a.dtype) for a in arrays],
        scratch_shapes=[pltpu.SemaphoreType.DMA((N_REL * n,)), pltpu.SemaphoreType.DMA((N_REL * n,)),
                        pltpu.SemaphoreType.DMA((n,))],
    )(*arrays)


HBM = pl.BlockSpec(memory_space=pltpu.HBM)
SEM = pl.BlockSpec(memory_space=pltpu.SEMAPHORE)
EFFECT = pltpu.SideEffectType.DATAFLOW_SIDE_EFFECTING


def _peer_of(x, y, c, k):
    flip = lambda v, bit: 1 - v if bit else v
    return flip(x, k & 4), flip(y, k & 2), flip(c, k & 1)


def _view_whole(src, slot):
    return src


def _view_near(src, slot):
    return src


_view_near.peers = (1, 2, 4, 6)


def _view_block(src, slot):
    return src.at[slot]


W_IN_SHARD = W_IN_REF // N_DEV


def _view_window(rows):
    def view(src, slot):
        col0 = pl.multiple_of((W_IN_SHARD * slot // DH) * DH, DH)
        return src.at[pl.ds(rows[0], rows[1] - rows[0]), pl.ds(col0, D)]
    return view


def _split_copies(view, srcs, lands, send_sems, recv_sems, local_sems):
    x, y, c = _place()
    me = _slot((x, y, c))
    local, sends, waits = [], [], []
    for a, (src, land) in enumerate(zip(srcs, lands)):
        local.append(pltpu.make_async_copy(view(src, me), land.at[me], local_sems.at[a]))
        for k in getattr(view, "peers", range(1, N_DEV)):
            peer = _peer_of(x, y, c, k)
            mine = view(src, _slot(peer))
            sems = dict(send_sem=send_sems.at[N_REL * a + k - 1], recv_sem=recv_sems.at[N_REL * a + k - 1],
                        device_id=peer, device_id_type=MESH)
            sends.append(pltpu.make_async_remote_copy(src_ref=mine, dst_ref=land.at[me], **sems))
            waits.append(pltpu.make_async_remote_copy(src_ref=mine, dst_ref=land.at[_slot(peer)], **sems))
    return local, sends, waits


def _split_start(view, land_shapes, srcs, name, after):
    n = len(srcs)
    lands = [lax.empty(shp, s.dtype) for shp, s in zip(land_shapes, srcs)]

    def body(*refs):
        src_refs, land_refs = refs[:n], refs[n:2 * n]
        send_sems, recv_sems, local_sems = refs[2 * n + 1:2 * n + 4]
        token = refs[-1]
        local, sends, _ = _split_copies(view, src_refs, land_refs, send_sems, recv_sems, local_sems)
        for cp in local + sends:
            cp.start()
        token[...] = jnp.zeros_like(token)

    hbm = lambda a: pltpu.with_memory_space_constraint(a, pltpu.HBM)
    out = pl.pallas_call(
        body, name=name,
        out_shape=(pltpu.SemaphoreType.DMA((N_REL * n,)), pltpu.SemaphoreType.DMA((N_REL * n,)),
                   pltpu.SemaphoreType.DMA((n,)),
                   *[pltpu.HBM(s.shape, s.dtype) for s in srcs], *[pltpu.HBM(l.shape, l.dtype) for l in lands],
                   jax.ShapeDtypeStruct((8, DH), F32)),
        in_specs=[HBM] * (2 * n) + [ANY],
        out_specs=(SEM, SEM, SEM, *([HBM] * (2 * n)), pl.BlockSpec(memory_space=pltpu.VMEM)),
        input_output_aliases={i: 3 + i for i in range(2 * n)},
        compiler_params=pltpu.CompilerParams(has_side_effects=EFFECT),
    )(*[hbm(s) for s in srcs], *[hbm(l) for l in lands], after)
    handle = dict(view=view, n=n, sems=out[:3], srcs=list(out[3:3 + n]), lands=list(out[3 + n:3 + 2 * n]))
    return handle, out[-1]


def _split_wait(handle, name, after, srcs=None):
    view, n, sems, lands = handle["view"], handle["n"], handle["sems"], handle["lands"]
    srcs = handle["srcs"] if srcs is None else srcs
    afters = list(after) if isinstance(after, (list, tuple)) else [after]

    def body(*refs):
        src_refs, land_refs = refs[:n], refs[n:2 * n]
        send_sems, recv_sems, local_sems = refs[2 * n:2 * n + 3]
        local, _, waits = _split_copies(view, src_refs, land_refs, send_sems, recv_sems, local_sems)
        for cp in waits:
            cp.wait_send()
            cp.wait_recv()
        for cp in local:
            cp.wait()

    out = pl.pallas_call(
        body, name=name,
        out_shape=(*[pltpu.HBM(s.shape, s.dtype) for s in srcs], *[pltpu.HBM(l.shape, l.dtype) for l in lands]),
        in_specs=[HBM] * (2 * n) + [SEM, SEM, SEM] + [ANY] * len(afters),
        out_specs=tuple([HBM] * (2 * n)),
        input_output_aliases={i: i for i in range(2 * n)},
        compiler_params=pltpu.CompilerParams(has_side_effects=EFFECT),
    )(*srcs, *lands, *sems, *afters)
    handle["srcs"] = list(out[:n])
    return list(out[n:])


def _tie(x, token, name):
    def body(x_ref, t_ref, o_ref):
        pass

    return pl.pallas_call(
        body, name=name, out_shape=jax.ShapeDtypeStruct(x.shape, x.dtype),
        in_specs=[ANY, ANY], out_specs=ANY, input_output_aliases={0: 0},
    )(x, token)


def _forward_to_sibling(land, name):
    def body(land_ref, out_ref, send_sems, recv_sems):
        x, y, c = _place()
        sibling = (x, y, 1 - c)
        chips = [(1 - x, y), (x, 1 - y), (1 - x, 1 - y)]

        def copy(j, core):
            blk = _slot((*chips[j], core))
            return pltpu.make_async_remote_copy(src_ref=land_ref.at[blk], dst_ref=out_ref.at[blk],
                                                send_sem=send_sems.at[j], recv_sem=recv_sems.at[j],
                                                device_id=sibling, device_id_type=MESH)

        sends = [copy(j, c) for j in range(3)]
        for cp in sends:
            cp.start()
        for j in range(3):
            copy(j, 1 - c).wait_recv()
        for cp in sends:
            cp.wait_send()

    return pl.pallas_call(
        body, name=name, in_specs=[ANY], out_specs=ANY, input_output_aliases={0: 0},
        out_shape=jax.ShapeDtypeStruct(land.shape, land.dtype),
        scratch_shapes=[pltpu.SemaphoreType.DMA((3,)), pltpu.SemaphoreType.DMA((3,))],
    )(land)


def _mod_fwd(a, w, b):
    def body(a_ref, w_ref, b_ref, o_ref):
        o_ref[...] = _dot(_silu(a_ref[...]), w_ref[...], NN, precision=HI) + b_ref[...]

    return pl.pallas_call(
        body, name="mod_fwd", out_shape=jax.ShapeDtypeStruct((a.shape[0], w.shape[1]), F32),
        compiler_params=pltpu.CompilerParams(vmem_limit_bytes=VMEM_LIMIT),
    )(a, w, b)


def _mod_bwd(a, d, w):
    def body(a_ref, d_ref, w_ref, dw_ref, dc_ref):
        av = a_ref[...]
        dv = d_ref[...]
        dw_ref[...] = _dot(_silu(av), dv, TN, precision=HI)
        da = _dot(dv[0:8, :], w_ref[...], NT, precision=HI) * _dsilu(av[0:8, :])
        row = lax.broadcasted_iota(jnp.int32, da.shape, 0)
        dc_ref[...] = jnp.where(row == 0, da, 0.0)

    return pl.pallas_call(
        body, name="mod_bwd",
        out_shape=[jax.ShapeDtypeStruct(w.shape, F32), jax.ShapeDtypeStruct((8, w.shape[0]), F32)],
        compiler_params=pltpu.CompilerParams(vmem_limit_bytes=VMEM_LIMIT),
    )(a, d, w)


def _sum_devices(g):
    def body(g_ref, o_ref):
        acc = g_ref[0]
        for i in range(1, g.shape[0]):
            acc = acc + g_ref[i]
        o_ref[...] = acc

    return pl.pallas_call(body, name="sum_devices_%d" % g.shape[1],
                          out_shape=jax.ShapeDtypeStruct(g.shape[1:], F32))(g)


def _sum_windows(g, name):
    n, r, c = g.shape
    tr = 128

    def body(g_ref, o_ref):
        x, y, cc = _place()
        lane0 = (W_IN_SHARD * _slot((x, y, cc))) % DH
        acc = g_ref[0].astype(F32)
        for i in range(1, n):
            acc = acc + g_ref[i].astype(F32)
        o_ref[...] = pltpu.roll(acc, (c - lane0) % c, 1).T

    return pl.pallas_call(
        body, name=name, grid=(r // tr,),
        in_specs=[pl.BlockSpec((n, tr, c), lambda i: (0, i, 0))],
        out_specs=pl.BlockSpec((c, tr), lambda i: (0, i)),
        out_shape=jax.ShapeDtypeStruct((c, r), F32),
        compiler_params=_cp("parallel"),
    )(g)


def _adam_rows(r, c, n):
    budget = 6 * 1024 * 1024
    best = None
    for tr in range(16, r + 1, 16):
        if r % tr == 0 and tr * c * (2 * n + 28) <= budget:
            best = tr
    return best if best is not None else r


def _adamw(g, w, m, v, name):
    n, r, c = g.shape
    tr = _adam_rows(r, c, n)
    bc1 = 1.0 - ADAM_B1 ** ADAM_STEP
    bc2 = 1.0 - ADAM_B2 ** ADAM_STEP

    def body(g_ref, w_ref, m_ref, v_ref, go_ref, d_ref, mo_ref, vo_ref):
        grad = g_ref[0].astype(F32)
        for i in range(1, n):
            grad = grad + g_ref[i].astype(F32)
        go_ref[...] = grad
        m_new = ADAM_B1 * m_ref[...] + (1.0 - ADAM_B1) * grad
        v_new = ADAM_B2 * v_ref[...] + (1.0 - ADAM_B2) * (grad * grad)
        mo_ref[...] = m_new
        vo_ref[...] = v_new
        d_ref[...] = -ADAM_LR * ((m_new / bc1) / (jnp.sqrt(v_new / bc2) + ADAM_EPS) + ADAM_WD * w_ref[...])

    blk = pl.BlockSpec((tr, c), lambda i: (i, 0))
    out = jax.ShapeDtypeStruct((r, c), F32)
    return pl.pallas_call(
        body, name=name, grid=(r // tr,),
        in_specs=[pl.BlockSpec((n, tr, c), lambda i: (0, i, 0)), blk, blk, blk],
        out_specs=[blk] * 4, out_shape=[out] * 4,
        compiler_params=_cp("parallel"),
    )(g, w, m, v)


ADAM_ROWS3 = 168


def _adam_math(grad, w, m, v):
    bc1 = 1.0 - ADAM_B1 ** ADAM_STEP
    bc2 = 1.0 - ADAM_B2 ** ADAM_STEP
    m_new = ADAM_B1 * m + (1.0 - ADAM_B1) * grad
    v_new = ADAM_B2 * v + (1.0 - ADAM_B2) * (grad * grad)
    delta = -ADAM_LR * ((m_new / bc1) / (jnp.sqrt(v_new / bc2) + ADAM_EPS) + ADAM_WD * w)
    return delta, m_new, v_new


def _adamw_rows3(g, w3, m3, v3, name):
    r, _, c = w3.shape
    n = ADAM_ROWS3
    starts = list(range(0, r - n, n)) + [r - n]

    def body(g_hbm, w_hbm, m_hbm, v_hbm, go_hbm, d_hbm, mo_hbm, vo_hbm, gbuf, ibuf, obuf, sems):
        for r0 in starts:
            g0 = (r0 // 8) * 8
            ins = [pltpu.make_async_copy(g_hbm.at[pl.ds(g0, n + 8)], gbuf, sems.at[0])]
            ins += [pltpu.make_async_copy(h.at[pl.ds(r0, n), 0], ibuf.at[k], sems.at[1 + k])
                    for k, h in enumerate((w_hbm, m_hbm, v_hbm))]
            for cp in ins:
                cp.start()
            for cp in ins:
                cp.wait()
            grad = gbuf[pl.ds(r0 - g0, n), :]
            delta, m_new, v_new = _adam_math(grad, ibuf[0], ibuf[1], ibuf[2])
            for k, val in enumerate((grad, delta, m_new, v_new)):
                obuf[k] = val
            outs = [pltpu.make_async_copy(obuf.at[k], h.at[pl.ds(r0, n), 0], sems.at[4 + k])
                    for k, h in enumerate((go_hbm, d_hbm, mo_hbm, vo_hbm))]
            for cp in outs:
                cp.start()
            for cp in outs:
                cp.wait()

    out = jax.ShapeDtypeStruct(w3.shape, F32)
    return pl.pallas_call(
        body, name=name, in_specs=[ANY] * 4, out_specs=[ANY] * 4, out_shape=[out] * 4,
        scratch_shapes=[pltpu.VMEM((n + 8, c), F32), pltpu.VMEM((3, n, c), F32), pltpu.VMEM((4, n, c), F32),
                        pltpu.SemaphoreType.DMA((8,))],
        compiler_params=pltpu.CompilerParams(vmem_limit_bytes=VMEM_LIMIT),
    )(g, w3, m3, v3)


def kernel(x, c, ctx, c_ctx, w_mod, b_mod, norm_pre1, norm_post1, norm_pre2, norm_post2, w_in, hg_lb, hg_onorm, gla_w_gk, gla_b_gk, gla_onorm, w_br_hg, w_br_gla, w_out, w_ff_gate, w_ff_up, w_ff_down, loss_target, m_c_ctx, m_w_mod, m_b_mod, m_norm_pre1, m_norm_post1, m_norm_pre2, m_norm_post2, m_w_in, m_hg_lb, m_hg_onorm, m_gla_w_gk, m_gla_b_gk, m_gla_onorm, m_w_br_hg, m_w_br_gla, m_w_out, m_w_ff_gate, m_w_ff_up, m_w_ff_down, v_c_ctx, v_w_mod, v_b_mod, v_norm_pre1, v_norm_post1, v_norm_pre2, v_norm_post2, v_w_in, v_hg_lb, v_hg_onorm, v_gla_w_gk, v_gla_b_gk, v_gla_onorm, v_w_br_hg, v_w_br_gla, v_w_out, v_w_ff_gate, v_w_ff_up, v_w_ff_down):
    xi, yi, ci = lax.axis_index("x"), lax.axis_index("y"), lax.axis_index("c")
    me = 4 * xi + 2 * yi + ci
    t = CTX + x.shape[1]

    c_all, lb_g, wgk_g, bgk_g = _all_gather([c, hg_lb, gla_w_gk[0], gla_b_gk[0]], "ag_small")
    tr_ = lambda a: jnp.swapaxes(a[0], 0, 1)
    big = [w_in[0], w_br_hg[0], w_br_gla[0], w_out[0], tr_(w_ff_gate), tr_(w_ff_up), w_ff_down[0]]
    big_bf = [w.astype(BF16) for w in big]
    big_bf[0] = jnp.pad(big_bf[0], ((0, 0), (0, W_IN_PAD - W_IN_SHARD)))
    cols = lambda g: jnp.transpose(g, (1, 0, 2)).reshape(g.shape[1], N_DEV * g.shape[2])

    def get_w_in(after):
        land, = _split_wait(w_in_handle, "ag_w_in_wait", after)
        return _assemble_w_in(_forward_to_sibling(land, "ag_w_in_forward"))

    def get_mix(after):
        g_brh, g_brg, g_out = _split_wait(mix_handle, "ag_mix_wait", after)
        return _gate_cols(cols(g_brh)), _gate_cols(cols(g_brg)), _gate_rows(g_out.reshape(D, D))

    def get_ffn(after):
        g_gate, g_up, g_down = _split_wait(ffn_handle, "ag_ffn_wait", after)
        return (g_gate.reshape(D_FF, D), g_up.reshape(D_FF, D)), g_down.reshape(D_FF, D)

    hg_lb_full = jnp.transpose(lb_g, (1, 2, 0, 3)).reshape(2, 2, HW)
    wgk_k = _layout_wgk(jnp.transpose(wgk_g, (1, 2, 0, 3)).reshape(2, 16, HW)).astype(BF16)
    bgk_k = jnp.transpose(bgk_g, (1, 0, 2)).reshape(1, D)
    onw = jnp.concatenate([jnp.tile(hg_onorm, (1, NH // 2)), jnp.tile(gla_onorm, (1, NH // 2))], axis=1)

    n_mod = w_mod.shape[2]
    a9 = jnp.concatenate([c_ctx[None], c_all[:, 0], jnp.zeros((16 - 1 - N_DEV, D), F32)], axis=0)
    b_loc = lax.dynamic_slice(b_mod, (0, me * n_mod), (1, n_mod))
    s_loc = _mod_fwd(a9, w_mod[0], b_loc)
    s_all, = _all_gather([s_loc], "ag_mod")
    mod_all = jnp.transpose(s_all, (1, 0, 2)).reshape(16, N_DEV * n_mod)
    pad8 = lambda m: jnp.concatenate([m.reshape(6, D), jnp.zeros((2, D), F32)], axis=0)
    modc = pad8(mod_all[0])
    modx = pad8(lax.dynamic_slice(mod_all, (1 + me, 0), (1, N_DEV * n_mod))[0])

    gathered = lambda arrs: [(N_DEV,) + a.shape for a in arrs]
    w_in_handle, tok = _split_start(_view_near, gathered(big_bf[:1]), big_bf[:1], "ag_w_in_start", s_all)
    mix_handle, tok = _split_start(_view_whole, gathered(big_bf[1:4]), big_bf[1:4], "ag_mix_start", tok)
    ffn_handle, tok = _split_start(_view_whole, gathered(big_bf[4:]), big_bf[4:], "ag_ffn_start", tok)

    z = (ctx[0], x[0])
    modx = _tie(modx, tok, "tie_mod")
    norms = (norm_pre1, norm_post1, norm_pre2, norm_post2)
    shard = lambda d: jnp.transpose(d.reshape(d.shape[0], N_DEV, -1), (1, 0, 2)).astype(BF16)
    rowshard = lambda d: d.reshape(N_DEV, d.shape[0] // N_DEV, d.shape[1]).astype(BF16)
    sent, w_in_grad = [], {}

    def send_w_in(i, x_after):
        half, rows = W_IN_GRAD_CHUNKS[i]
        handle, tok = _split_start(_view_window(rows), [(N_DEV, rows[1] - rows[0], D)], w_in_grad[half],
                                   "grads_w_in%d_start" % i, x_after)
        w_in_grad[half] = handle["srcs"]
        sent.append(("w_in%d" % i, ["w_in#%d" % i], handle))
        return tok

    def send(names, grads, x_after):
        if names == ("w_in_a",):
            w_in_grad["a"] = list(grads)
            return _tie(x_after, send_w_in(0, x_after), "tie_w_in0")
        if names == ("w_in_b",):
            w_in_grad["b"] = list(grads)
            return x_after
        arrs, leaves = [], []
        for nm, g in zip(names, grads):
            if nm in ("w_gate_t", "w_up_t"):
                arrs.append(rowshard(g))
                leaves.append({"w_gate_t": "w_ff_gate", "w_up_t": "w_ff_up"}[nm])
            elif nm == "w_down":
                arrs.append(rowshard(g))
                leaves.append("w_ff_down")
            elif nm == "w_out":
                arrs.append(rowshard(g[GOFF:GOFF + D]))
                leaves.append(nm)
            else:
                arrs.append(shard(g[:, GOFF:GOFF + D]))
                leaves.append(nm)
        handle, tok = _split_start(_view_block, [a.shape for a in arrs], arrs, "grads_%s_start" % names[0], x_after)
        sent.append((names[0], leaves, handle))
        return _tie(x_after, tok, "tie_" + names[0])

    r = _local_step(z, loss_target[0], modc, modx, norms, onw, hg_lb_full, wgk_k, bgk_k,
                    get_w_in, get_mix, get_ffn, send)
    grad_x = r["grad_x"][None]

    sm_pre, sm_mid, sm_fin = r["sm_pre"], r["sm_mid"], r["sm_final"]
    dmodc = jnp.stack([sm_pre[0], sm_pre[2], sm_mid[4], sm_mid[0], sm_mid[2], sm_fin[0]]).reshape(-1)
    dmodx = jnp.stack([sm_pre[1], sm_pre[3], sm_mid[5], sm_mid[1], sm_mid[3], sm_fin[1]]).reshape(-1)
    on = r["sm_post"][0].reshape(NH, DH)
    pieces = [dmodc, dmodx, sm_pre[4], sm_mid[7], sm_mid[6], sm_fin[2], on[:NH // 2].sum(0), on[NH // 2:].sum(0),
              r["d_lb"][:2].reshape(-1), _unlayout_wgk(r["d_wgk"]).reshape(-1), r["d_bgk"][0]]
    loss_local = (0.5 / D) * jnp.sum(r["loss_vec"])
    pieces.append(jnp.concatenate([loss_local.reshape(1), jnp.zeros((DH - 1,), F32)]))
    sizes = [p.shape[0] for p in pieces]
    pack = jnp.concatenate(pieces).reshape(-1, DH)
    moms = [(m_w_in, v_w_in), (m_w_br_hg, v_w_br_hg), (m_w_br_gla, v_w_br_gla), (m_w_out, v_w_out),
            (m_w_ff_gate, v_w_ff_gate), (m_w_ff_up, v_w_ff_up), (m_w_ff_down, v_w_ff_down)]
    names = ["w_in", "w_br_hg", "w_br_gla", "w_out", "w_ff_gate", "w_ff_up", "w_ff_down"]
    wmv = {nm: (w, m, v) for nm, w, (m, v) in zip(names, big, moms)}
    res = {}

    def update(nm):
        w, m, v = wmv[nm]
        if nm in ("w_ff_gate", "w_ff_up"):
            outs = _adamw(recv[nm], w, tr_(m), tr_(v), "adamw_" + nm)
            res[nm] = [jnp.swapaxes(o, 0, 1)[None] for o in outs]
        else:
            res[nm] = [o[None] for o in _adamw(recv[nm], w, m[0], v[0], "adamw_" + nm)]

    small_handle, tok = _split_start(_view_whole, [(N_DEV,) + pack.shape], [pack], "small_grads_start", pack)
    tok = send_w_in(1, tok)
    recv = {}
    for first, leaves, handle in sent:
        if not first.startswith("w_in"):
            recv.update(zip(leaves, _split_wait(handle, "grads_%s_wait" % first, tok)))
    update("w_ff_gate")
    update("w_ff_up")
    pack_all, = _split_wait(small_handle, "small_grads_wait", [res["w_ff_gate"][0], res["w_ff_up"][0]])
    tot = _sum_devices(pack_all).reshape(-1)
    offs = [sum(sizes[:i]) for i in range(len(sizes))]
    part = lambda i: tot[offs[i]:offs[i] + sizes[i]]
    dmodc_t, dmodx_t = part(0), part(1)
    g_b_mod = (dmodc_t + dmodx_t)[None]
    g_norms = [part(i)[None] for i in (2, 3, 4, 5)]
    g_hg_on, g_gla_on = part(6)[None], part(7)[None]
    lb0 = lax.dynamic_slice(part(8).reshape(2, HW), (0, me * (HW // N_DEV)), (2, HW // N_DEV))
    g_hg_lb = jnp.stack([lb0, -lb0])
    g_wgk = lax.dynamic_slice(part(9).reshape(2, 16, HW), (0, 0, me * (HW // N_DEV)), (2, 16, HW // N_DEV))[None]
    g_bgk = lax.dynamic_slice(part(10).reshape(2, HW), (0, me * (HW // N_DEV)), (2, HW // N_DEV))[None]
    loss = part(11)[0]

    dmx_all = pack_all.reshape(N_DEV, -1)[:, sizes[0]:sizes[0] + sizes[1]]
    d9 = jnp.concatenate([lax.dynamic_slice(dmodc_t[None], (0, me * n_mod), (1, n_mod)),
                          lax.dynamic_slice(dmx_all, (0, me * n_mod), (N_DEV, n_mod)),
                          jnp.zeros((16 - 1 - N_DEV, n_mod), F32)], axis=0)
    g_w_mod, dcc_part = _mod_bwd(a9, d9, w_mod[0])
    cctx_handle, tok = _split_start(_view_whole, [(N_DEV,) + dcc_part.shape], [dcc_part], "c_ctx_start", dcc_part)
    tok = send_w_in(2, tok)
    recv["w_ff_down"] = _tie(recv["w_ff_down"], tok, "tie_down")
    update("w_ff_down")
    res["w_mod"] = [o[None] for o in _adamw(g_w_mod[None], w_mod[0], m_w_mod[0], v_w_mod[0], "adamw_w_mod")]
    for nm in ("w_out", "w_br_hg", "w_br_gla"):
        update(nm)
    dcc_all, = _split_wait(cctx_handle, "c_ctx_wait", [res["w_ff_down"][0], res["w_mod"][0]])
    g_c_ctx = _sum_devices(dcc_all)[0]

    small = [("c_ctx", c_ctx, m_c_ctx, v_c_ctx, g_c_ctx), ("b_mod", b_mod, m_b_mod, v_b_mod, g_b_mod),
             ("norm_pre1", norm_pre1, m_norm_pre1, v_norm_pre1, g_norms[0]),
             ("norm_post1", norm_post1, m_norm_post1, v_norm_post1, g_norms[1]),
             ("norm_pre2", norm_pre2, m_norm_pre2, v_norm_pre2, g_norms[2]),
             ("norm_post2", norm_post2, m_norm_post2, v_norm_post2, g_norms[3]),
             ("hg_lb", hg_lb, m_hg_lb, v_hg_lb, g_hg_lb), ("hg_onorm", hg_onorm, m_hg_onorm, v_hg_onorm, g_hg_on),
             ("gla_w_gk", gla_w_gk, m_gla_w_gk, v_gla_w_gk, g_wgk), ("gla_b_gk", gla_b_gk, m_gla_b_gk, v_gla_b_gk, g_bgk),
             ("gla_onorm", gla_onorm, m_gla_onorm, v_gla_onorm, g_gla_on)]
    flat = lambda k: jnp.concatenate([s[k].reshape(-1) for s in small]).reshape(-1, DH)
    outs = _adamw(flat(4)[None], flat(1), flat(2), flat(3), "adamw_small")
    off = 0
    for nm, w, _, _, _ in small:
        res[nm] = [o.reshape(-1)[off:off + w.size].reshape(w.shape) for o in outs]
        off += w.size

    done = [res[nm][0] for nm in names[1:]] + [res["w_mod"][0], outs[0]]
    sums = []
    for i, (first, leaves, handle) in enumerate(s for s in sent if s[0].startswith("w_in")):
        half = W_IN_GRAD_CHUNKS[i][0]
        land, = _split_wait(handle, "grads_%s_wait" % first, done, srcs=w_in_grad[half])
        w_in_grad[half] = handle["srcs"]
        sums.append(_sum_windows(land, "sum_windows%d" % i))
    major = lambda a: jnp.transpose(a, (2, 0, 1))
    outs = _adamw_rows3(jnp.concatenate(sums, axis=1), major(w_in), major(m_w_in), major(v_w_in), "adamw_w_in")
    res["w_in"] = [jnp.transpose(o, (1, 2, 0)) for o in outs]

    order = ["c_ctx", "w_mod", "b_mod", "norm_pre1", "norm_post1", "norm_pre2", "norm_post2", "w_in", "hg_lb",
             "hg_onorm", "gla_w_gk", "gla_b_gk", "gla_onorm", "w_br_hg", "w_br_gla", "w_out", "w_ff_gate", "w_ff_up",
             "w_ff_down"]
    return (loss, grad_x, *[res[n][k] for k in range(4) for n in order])
```

```python
import functools

import jax
import jax.numpy as jnp
from jax import lax
from jax.experimental import pallas as pl
from jax.experimental.pallas import tpu as pltpu

F32 = jnp.float32
BF16 = jnp.bfloat16
HI = lax.Precision.HIGHEST

N_DEV = 8
D = 1024
CTX = 256
HW = 512
DH = 128
NH = 8
D_FF = 2816
EPS = 1e-6
GLA_NORM = 16.0
CHUNK = 64
TR = 256
NCT = CTX // TR
W_IN_COLS = 7168
MAIN0 = 0
LR0 = 4608
GW = 1152
GOFF = 32
GATE_HG0 = LR0
GATE_GLA0 = LR0 + D
LEVELS = (32, 16, 8)
EXP_CLAMP = 80.0
VMEM_LIMIT = 48 * 1024 * 1024

ADAM_LR, ADAM_B1, ADAM_B2, ADAM_EPS, ADAM_WD, ADAM_STEP = 0.001, 0.9, 0.999, 1e-08, 0.01, 10


def _cp(*sem):
    return pltpu.CompilerParams(dimension_semantics=sem, vmem_limit_bytes=VMEM_LIMIT)


def _sig(x):
    return jax.nn.sigmoid(x)


def _silu(x):
    return x * _sig(x)


def _dsilu(x):
    s = _sig(x)
    return s * (1.0 + x * (1.0 - s))


def _rstd(x):
    return lax.rsqrt(jnp.mean(x * x, axis=-1, keepdims=True) + EPS)


def _rms_bwd(a, y, r):
    return r * (a - y * (r * r) * jnp.mean(a * y, axis=-1, keepdims=True))


def _colsum(x):
    return jnp.sum(x, axis=0, keepdims=True)


def _dot(a, b, dims, precision=None):
    return lax.dot_general(a, b, (dims, ((), ())), preferred_element_type=F32, precision=precision)


NN = ((1,), (0,))
NT = ((1,), (1,))
TN = ((0,), (0,))

SCAN_HEADS_FWD = 4
SCAN_HEADS_BWD = 4


def _split_dot(m, x):
    mb = m.astype(BF16)
    x1 = x.astype(BF16)
    r1 = x - x1.astype(F32)
    x2 = r1.astype(BF16)
    x3 = (r1 - x2.astype(F32)).astype(BF16)
    return _dot(mb, x1, NN) + _dot(mb, x2, NN) + _dot(mb, x3, NN)


def _matmul(a, b, dims, out_dtype, name, tm, tn, tk, a_off=0, m_out=None):
    pair = isinstance(b, (tuple, list))
    bs = list(b) if pair else [b]
    b1 = bs[0]
    rows = b1.shape[0] * len(bs)
    half = None
    if dims == NN:
        m, k, n = a.shape[0], rows, b1.shape[1]
        a_spec = pl.BlockSpec((tm, tk), lambda i, j, kk: (i, kk + a_off))
        half = b1.shape[0] // tk
        b_maps = [lambda i, j, kk: (kk, j)] if not pair else [
            lambda i, j, kk: (jnp.minimum(kk, half - 1), j), lambda i, j, kk: (jnp.maximum(kk - half, 0), j)]
        b_specs = [pl.BlockSpec((tk, tn), f) for f in b_maps]
        axis = 2
    elif dims == NT:
        m, k, n = a.shape[0], b1.shape[1], rows
        a_spec = pl.BlockSpec((tm, tk), lambda i, j, kk: (i, kk + a_off))
        half = b1.shape[0] // tn
        b_maps = [lambda i, j, kk: (j, kk)] if not pair else [
            lambda i, j, kk: (jnp.minimum(j, half - 1), kk), lambda i, j, kk: (jnp.maximum(j - half, 0), kk)]
        b_specs = [pl.BlockSpec((tn, tk), f) for f in b_maps]
        axis = 1
    else:
        assert not pair
        m, k = (a.shape[1] if m_out is None else m_out), a.shape[0]
        n = b1.shape[1]
        a_spec = pl.BlockSpec((tk, tm), lambda i, j, kk: (kk, i + a_off))
        b_specs = [pl.BlockSpec((tk, tn), lambda i, j, kk: (kk, j))]
    assert m % tm == 0 and n % tn == 0 and k % tk == 0, (name, m, n, k, tm, tn, tk)
    nk = k // tk
    nb = len(bs)

    def body(a_ref, *refs):
        o_ref = refs[nb]
        if pair:
            bv = jnp.where(pl.program_id(axis) < half, refs[0][...], refs[1][...])
        else:
            bv = refs[0][...]
        part = _dot(a_ref[...], bv, dims)
        if nk == 1:
            o_ref[...] = part.astype(o_ref.dtype)
            return
        acc_ref = refs[nb + 1]
        kk = pl.program_id(2)

        @pl.when(kk == 0)
        def _():
            acc_ref[...] = part

        @pl.when(kk > 0)
        def _():
            acc_ref[...] += part

        @pl.when(kk == nk - 1)
        def _():
            o_ref[...] = acc_ref[...].astype(o_ref.dtype)

    return pl.pallas_call(
        body,
        name=name,
        grid=(m // tm, n // tn, nk),
        in_specs=[a_spec] + b_specs,
        out_specs=pl.BlockSpec((tm, tn), lambda i, j, kk: (i, j)),
        out_shape=jax.ShapeDtypeStruct((m, n), out_dtype),
        scratch_shapes=[] if nk == 1 else [pltpu.VMEM((tm, tn), F32)],
        compiler_params=_cp("parallel", "parallel", "arbitrary"),
    )(a, *bs)


def _row(c):
    return pl.BlockSpec((TR, c), lambda i: (i, 0))


def _rowcol(width, cb):
    return pl.BlockSpec((TR, width), lambda i: (i, cb))


def _full(shape):
    return pl.BlockSpec(shape, lambda i: (0,) * len(shape))


def _mod_row(mc_ref, mx_ref, k, is_ctx):
    return jnp.where(is_ctx, mc_ref[k:k + 1, :], mx_ref[k:k + 1, :])


def _z_specs():
    return [pl.BlockSpec((TR, D), lambda i: (jnp.minimum(i, NCT - 1), 0)),
            pl.BlockSpec((TR, D), lambda i: (jnp.maximum(i - NCT, 0), 0))]


def _z_tile(c_ref, x_ref, is_ctx):
    return jnp.where(is_ctx, c_ref[...], x_ref[...])


def _acc_row(ref, k, val):
    ref[k:k + 1, :] += val


def _acc_mod(ref, k, is_ctx, val):
    zero = jnp.zeros_like(val)
    ref[k:k + 1, :] += jnp.where(is_ctx, val, zero)
    ref[k + 1:k + 2, :] += jnp.where(is_ctx, zero, val)


def _prenorm(z, nw, modc, modx, i_shift, i_scale, name):
    t = z[0].shape[0] + z[1].shape[0]

    def body(zc_ref, zx_ref, nw_ref, mc_ref, mx_ref, h_ref):
        is_ctx = pl.program_id(0) < NCT
        x = _z_tile(zc_ref, zx_ref, is_ctx)
        n = x * _rstd(x) * nw_ref[...]
        h = n * (1.0 + _mod_row(mc_ref, mx_ref, i_scale, is_ctx)) + _mod_row(mc_ref, mx_ref, i_shift, is_ctx)
        h_ref[...] = h.astype(BF16)

    return pl.pallas_call(
        body, name=name, grid=(t // TR,),
        in_specs=_z_specs() + [_full((1, D)), _full((8, D)), _full((8, D))],
        out_specs=_row(D),
        out_shape=jax.ShapeDtypeStruct((t, D), BF16),
        compiler_params=_cp("parallel"),
    )(*z, nw, modc, modx)


def _hg_lb(lb_ref, d):
    a0 = lb_ref[0, d:d + 1, :]
    a1 = lb_ref[1, d:d + 1, :]
    mx = jnp.maximum(a0, a1)
    e0 = jnp.exp(a0 - mx)
    e1 = jnp.exp(a1 - mx)
    return e0 / (e0 + e1)


def _log_sigmoid(x):
    return jnp.minimum(x, 0.0) - jnp.log(1.0 + jnp.exp(-jnp.abs(x)))


def _gates_fwd(p, hg_lb, wgk, bgk):
    t = p.shape[0]
    seg = lambda j: _rowcol(HW, MAIN0 // HW + j)

    def body(hq_ref, hi_ref, hf_ref, hb_ref, gq_ref, gk_ref, gv_ref, lr_ref, lb_ref, wgk_ref, bgk_ref,
             q_ref, v_ref, kf_ref, kb_ref, gf_ref, gb_ref):
        q_ref[:, :HW] = _silu(hq_ref[...].astype(F32)).astype(BF16)
        q_ref[:, HW:] = (gq_ref[...].astype(F32) * (DH ** -0.5)).astype(BF16)
        v_ref[:, :HW] = hi_ref[...]
        v_ref[:, HW:] = gv_ref[...]
        xg = _dot(lr_ref[...].astype(BF16), wgk_ref[...], NN) + bgk_ref[...]
        for d, (raw_ref, k_ref, g_ref) in enumerate(((hf_ref, kf_ref, gf_ref), (hb_ref, kb_ref, gb_ref))):
            lbd = _hg_lb(lb_ref, d)
            f = lbd + (1.0 - lbd) * _sig(raw_ref[...].astype(F32))
            k_ref[:, :HW] = (1.0 - f).astype(BF16)
            k_ref[:, HW:] = gk_ref[...]
            g_ref[:, :HW] = jnp.log(f)
            g_ref[:, HW:] = _log_sigmoid(xg[:, d * HW:(d + 1) * HW]) * (1.0 / GLA_NORM)

    out = jax.ShapeDtypeStruct((t, D), F32)
    outb = jax.ShapeDtypeStruct((t, D), BF16)
    return pl.pallas_call(
        body, name="gates_fwd", grid=(t // TR,),
        in_specs=[seg(0), seg(1), seg(2), seg(3), seg(5), seg(6), seg(7), _rowcol(DH, LR0 // DH),
                  _full((2, 2, HW)), _full((DH, D)), _full((1, D))],
        out_specs=[_row(D)] * 6,
        out_shape=[outb] * 4 + [out] * 2,
        compiler_params=_cp("parallel"),
    )(p, p, p, p, p, p, p, p, hg_lb, wgk, bgk)


def _post_fwd(o_fw, o_bw, p, onw):
    t = o_fw.shape[0]

    def body(of_ref, ob_ref, g1_ref, g2_ref, w_ref, y_ref):
        for h in range(NH):
            sl = slice(h * DH, (h + 1) * DH)
            o = of_ref[:, sl] + ob_ref[:, sl]
            g_ref = g1_ref if h < NH // 2 else g2_ref
            gs = slice((h % (NH // 2)) * DH, (h % (NH // 2) + 1) * DH)
            n = o * _rstd(o) * w_ref[:, sl]
            y_ref[:, sl] = (n * _silu(g_ref[:, gs].astype(F32))).astype(BF16)

    return pl.pallas_call(
        body, name="post_fwd", grid=(t // TR,),
        in_specs=[_row(D), _row(D), _rowcol(HW, MAIN0 // HW + 4), _rowcol(HW, MAIN0 // HW + 8), _full((1, D))],
        out_specs=_row(D),
        out_shape=jax.ShapeDtypeStruct((t, D), BF16),
        compiler_params=_cp("parallel"),
    )(o_fw, o_bw, p, p, onw)


def _gate_window_specs(col0):
    return [_rowcol(HW, col0 // HW), _rowcol(HW, col0 // HW + 1), _rowcol(DH, (col0 + 2 * HW) // DH)]


def _gate_window(refs):
    return jnp.concatenate([r[...].astype(F32) for r in refs], axis=1)


def _merge_fwd(p, u1, u2):
    t = p.shape[0]

    def body(a0, a1, a2, b0, b1, b2, u1_ref, u2_ref, m_ref):
        f = lambda r: r[...].astype(F32)
        m_ref[...] = (_sig(_gate_window((a0, a1, a2))) * f(u1_ref)
                      + _sig(_gate_window((b0, b1, b2))) * f(u2_ref)).astype(BF16)

    return pl.pallas_call(
        body, name="merge_fwd", grid=(t // TR,),
        in_specs=_gate_window_specs(GATE_HG0) + _gate_window_specs(GATE_GLA0) + [_row(GW), _row(GW)],
        out_specs=_row(GW),
        out_shape=jax.ShapeDtypeStruct((t, GW), BF16),
        compiler_params=_cp("parallel"),
    )(p, p, p, p, p, p, u1, u2)


def _mid_fwd(z, y1, nw_post, nw_pre, modc, modx):
    t = y1.shape[0]

    def body(zc_ref, zx_ref, y_ref, wpo_ref, wpr_ref, mc_ref, mx_ref, z1_ref, h_ref):
        is_ctx = pl.program_id(0) < NCT
        y = y_ref[...].astype(F32)
        z1 = _z_tile(zc_ref, zx_ref, is_ctx) + _mod_row(mc_ref, mx_ref, 2, is_ctx) * (y * _rstd(y) * wpo_ref[...])
        z1_ref[...] = z1
        n = z1 * _rstd(z1) * wpr_ref[...]
        h = n * (1.0 + _mod_row(mc_ref, mx_ref, 4, is_ctx)) + _mod_row(mc_ref, mx_ref, 3, is_ctx)
        h_ref[...] = h.astype(BF16)

    return pl.pallas_call(
        body, name="mid_fwd", grid=(t // TR,),
        in_specs=_z_specs() + [_row(D), _full((1, D)), _full((1, D)), _full((8, D)), _full((8, D))],
        out_specs=[_row(D), _row(D)],
        out_shape=[jax.ShapeDtypeStruct((t, D), F32), jax.ShapeDtypeStruct((t, D), BF16)],
        compiler_params=_cp("parallel"),
    )(*z, y1, nw_post, nw_pre, modc, modx)


def _swiglu_fwd(uv):
    t = uv.shape[0]

    def body(u_ref, v_ref, a_ref):
        a_ref[...] = (_silu(u_ref[...].astype(F32)) * v_ref[...].astype(F32)).astype(BF16)

    return pl.pallas_call(
        body, name="swiglu_fwd", grid=(t // TR,),
        in_specs=[_rowcol(D_FF, 0), _rowcol(D_FF, 1)],
        out_specs=_row(D_FF),
        out_shape=jax.ShapeDtypeStruct((t, D_FF), BF16),
        compiler_params=_cp("parallel"),
    )(uv, uv)


def _swiglu_bwd(uv, da):
    t = uv.shape[0]

    def body(u_ref, v_ref, da_ref, d_ref):
        u = u_ref[...].astype(F32)
        d = da_ref[...].astype(F32)
        d_ref[:, :D_FF] = (d * v_ref[...].astype(F32) * _dsilu(u)).astype(BF16)
        d_ref[:, D_FF:] = (d * _silu(u)).astype(BF16)

    return pl.pallas_call(
        body, name="swiglu_bwd", grid=(t // TR,),
        in_specs=[_rowcol(D_FF, 0), _rowcol(D_FF, 1), _row(D_FF)],
        out_specs=_row(2 * D_FF),
        out_shape=jax.ShapeDtypeStruct((t, 2 * D_FF), BF16),
        compiler_params=_cp("parallel"),
    )(uv, uv, da)


def _final(z1, y2, target, nw, modc, modx):
    t = z1.shape[0]

    def body(z1_ref, y_ref, tg_ref, w_ref, mc_ref, mx_ref, dz_ref, dy_ref, loss_ref, sm_ref):
        i = pl.program_id(0)
        is_ctx = i < NCT

        @pl.when(i == 0)
        def _():
            loss_ref[...] = jnp.zeros_like(loss_ref)
            sm_ref[...] = jnp.zeros_like(sm_ref)

        g = _mod_row(mc_ref, mx_ref, 5, is_ctx)
        y = y_ref[...].astype(F32)
        r = _rstd(y)
        w = w_ref[...]
        yr = y * r
        n = yr * w
        e = z1_ref[...] + g * n - tg_ref[...]
        lat = jnp.where(is_ctx, 0.0, 1.0)
        loss_ref[...] += lat * _colsum(e * e)
        dz = e * (lat / D)
        dz_ref[...] = dz
        _acc_mod(sm_ref, 0, is_ctx, _colsum(dz * n))
        dn = dz * g
        _acc_row(sm_ref, 2, _colsum(dn * yr))
        dy_ref[...] = _rms_bwd(dn * w, y, r).astype(BF16)

    return pl.pallas_call(
        body, name="final", grid=(t // TR,),
        in_specs=[_row(D), _row(D), pl.BlockSpec((TR, D), lambda i: (jnp.maximum(i - NCT, 0), 0)),
                  _full((1, D)), _full((8, D)), _full((8, D))],
        out_specs=[_row(D), _row(D), _full((1, D)), _full((8, D))],
        out_shape=[jax.ShapeDtypeStruct((t, D), F32), jax.ShapeDtypeStruct((t, D), BF16),
                   jax.ShapeDtypeStruct((1, D), F32), jax.ShapeDtypeStruct((8, D), F32)],
        compiler_params=_cp("arbitrary"),
    )(z1, y2, target, nw, modc, modx)


def _mid_bwd(dh2, dz, z1, y1, nw_post, nw_pre, modc, modx):
    t = z1.shape[0]

    def body(dh_ref, dz_ref, z1_ref, y_ref, wpo_ref, wpr_ref, mc_ref, mx_ref, dzo_ref, dy_ref, sm_ref):
        i = pl.program_id(0)
        is_ctx = i < NCT

        @pl.when(i == 0)
        def _():
            sm_ref[...] = jnp.zeros_like(sm_ref)

        dh = dh_ref[...].astype(F32)
        z1 = z1_ref[...]
        r = _rstd(z1)
        zr = z1 * r
        wpr = wpr_ref[...]
        n = zr * wpr
        _acc_mod(sm_ref, 0, is_ctx, _colsum(dh))
        _acc_mod(sm_ref, 2, is_ctx, _colsum(dh * n))
        dn = dh * (1.0 + _mod_row(mc_ref, mx_ref, 4, is_ctx))
        _acc_row(sm_ref, 6, _colsum(dn * zr))
        dz1 = dz_ref[...] + _rms_bwd(dn * wpr, z1, r)
        dzo_ref[...] = dz1
        y = y_ref[...].astype(F32)
        r1 = _rstd(y)
        yr = y * r1
        wpo = wpo_ref[...]
        g = _mod_row(mc_ref, mx_ref, 2, is_ctx)
        _acc_mod(sm_ref, 4, is_ctx, _colsum(dz1 * (yr * wpo)))
        dn1 = dz1 * g
        _acc_row(sm_ref, 7, _colsum(dn1 * yr))
        dy_ref[...] = _rms_bwd(dn1 * wpo, y, r1).astype(BF16)

    return pl.pallas_call(
        body, name="mid_bwd", grid=(t // TR,),
        in_specs=[_row(D)] * 4 + [_full((1, D)), _full((1, D)), _full((8, D)), _full((8, D))],
        out_specs=[_row(D), _row(D), _full((8, D))],
        out_shape=[jax.ShapeDtypeStruct((t, D), F32), jax.ShapeDtypeStruct((t, D), BF16),
                   jax.ShapeDtypeStruct((8, D), F32)],
        compiler_params=_cp("arbitrary"),
    )(dh2, dz, z1, y1, nw_post, nw_pre, modc, modx)


def _pre_bwd(dh1, dz, z, nw, modc, modx):
    t = dh1.shape[0]

    def body(dh_ref, dz_ref, zc_ref, zx_ref, w_ref, mc_ref, mx_ref, dzo_ref, sm_ref):
        i = pl.program_id(0)
        is_ctx = i < NCT

        @pl.when(i == 0)
        def _():
            sm_ref[...] = jnp.zeros_like(sm_ref)

        dh = dh_ref[...].astype(F32)
        x = _z_tile(zc_ref, zx_ref, is_ctx)
        r = _rstd(x)
        xr = x * r
        w = w_ref[...]
        _acc_mod(sm_ref, 0, is_ctx, _colsum(dh))
        _acc_mod(sm_ref, 2, is_ctx, _colsum(dh * (xr * w)))
        dn = dh * (1.0 + _mod_row(mc_ref, mx_ref, 1, is_ctx))
        _acc_row(sm_ref, 4, _colsum(dn * xr))
        dzo_ref[...] = dz_ref[...] + _rms_bwd(dn * w, x, r)

    return pl.pallas_call(
        body, name="pre_bwd", grid=(t // TR,),
        in_specs=[_row(D)] * 2 + _z_specs() + [_full((1, D)), _full((8, D)), _full((8, D))],
        out_specs=[pl.BlockSpec((TR, D), lambda i: (jnp.maximum(i - NCT, 0), 0)), _full((8, D))],
        out_shape=[jax.ShapeDtypeStruct((t - CTX, D), F32), jax.ShapeDtypeStruct((8, D), F32)],
        compiler_params=_cp("arbitrary"),
    )(dh1, dz, *z, nw, modc, modx)


def _merge_bwd(dm, p, u1, u2):
    t = dm.shape[0]

    def body(dm_ref, a0, a1, a2, b0, b1, b2, u1_ref, u2_ref, du1_ref, du2_ref, dg_ref):
        dm_ = dm_ref[...].astype(F32)
        s1 = _sig(_gate_window((a0, a1, a2)))
        s2 = _sig(_gate_window((b0, b1, b2)))
        du1_ref[...] = (dm_ * s1).astype(BF16)
        du2_ref[...] = (dm_ * s2).astype(BF16)
        dg_ref[:, :GW] = (dm_ * u1_ref[...].astype(F32) * s1 * (1.0 - s1)).astype(BF16)
        dg_ref[:, GW:] = (dm_ * u2_ref[...].astype(F32) * s2 * (1.0 - s2)).astype(BF16)

    return pl.pallas_call(
        body, name="merge_bwd", grid=(t // TR,),
        in_specs=[_row(GW)] + _gate_window_specs(GATE_HG0) + _gate_window_specs(GATE_GLA0) + [_row(GW), _row(GW)],
        out_specs=[_row(GW), _row(GW), _row(2 * GW)],
        out_shape=[jax.ShapeDtypeStruct((t, GW), BF16), jax.ShapeDtypeStruct((t, GW), BF16),
                   jax.ShapeDtypeStruct((t, 2 * GW), BF16)],
        compiler_params=_cp("parallel"),
    )(dm, p, p, p, p, p, p, u1, u2)


def _post_bwd(dy_hg, dy_gla, o_fw, o_bw, p, onw):
    t = o_fw.shape[0]

    def body(d1_ref, d2_ref, of_ref, ob_ref, g1_ref, g2_ref, w_ref, do_ref, dg_ref, sm_ref):
        @pl.when(pl.program_id(0) == 0)
        def _():
            sm_ref[...] = jnp.zeros_like(sm_ref)

        for h in range(NH):
            sl = slice(h * DH, (h + 1) * DH)
            gs = slice((h % (NH // 2)) * DH, (h % (NH // 2) + 1) * DH)
            g_ref, d_ref = (g1_ref, d1_ref) if h < NH // 2 else (g2_ref, d2_ref)
            o = of_ref[:, sl] + ob_ref[:, sl]
            r = _rstd(o)
            orr = o * r
            w = w_ref[:, sl]
            gt = g_ref[:, gs].astype(F32)
            dy = d_ref[:, gs].astype(F32)
            dg_ref[:, sl] = (dy * (orr * w) * _dsilu(gt)).astype(BF16)
            dn = dy * _silu(gt)
            sm_ref[0:1, sl] += _colsum(dn * orr)
            do_ref[:, sl] = _rms_bwd(dn * w, o, r)

    return pl.pallas_call(
        body, name="post_bwd", grid=(t // TR,),
        in_specs=[_row(HW), _row(HW), _row(D), _row(D), _rowcol(HW, MAIN0 // HW + 4), _rowcol(HW, MAIN0 // HW + 8),
                  _full((1, D))],
        out_specs=[_row(D), _row(D), _full((8, D))],
        out_shape=[jax.ShapeDtypeStruct((t, D), F32), jax.ShapeDtypeStruct((t, D), BF16),
                   jax.ShapeDtypeStruct((8, D), F32)],
        compiler_params=_cp("arbitrary"),
    )(dy_hg, dy_gla, o_fw, o_bw, p, p, onw)


def _gates_bwd(p, hg_lb, wgk, bgk, dgm, dgo, dq_f, dq_b, dv_f, dv_b, dk_f, dk_b, dg_f, dg_b):
    t = p.shape[0]
    seg = lambda j: _rowcol(HW, MAIN0 // HW + j)

    def body(hq_ref, hf_ref, hb_ref, lr_ref, lb_ref, wgk_ref, bgk_ref, dgm_ref, dgo_ref,
             dqf_ref, dqb_ref, dvf_ref, dvb_ref, dkf_ref, dkb_ref, dgf_ref, dgb_ref,
             dp_ref, dlb_ref, dw_ref, db_ref):
        @pl.when(pl.program_id(0) == 0)
        def _():
            dlb_ref[...] = jnp.zeros_like(dlb_ref)
            dw_ref[...] = jnp.zeros_like(dw_ref)
            db_ref[...] = jnp.zeros_like(db_ref)

        c0 = MAIN0

        def put(j, val):
            dp_ref[:, c0 + j * HW:c0 + (j + 1) * HW] = val.astype(BF16)

        dq = dqf_ref[...].astype(F32) + dqb_ref[...].astype(F32)
        dv = dvf_ref[...].astype(F32) + dvb_ref[...].astype(F32)
        put(0, dq[:, :HW] * _dsilu(hq_ref[...].astype(F32)))
        put(1, dv[:, :HW])
        put(5, dq[:, HW:] * (DH ** -0.5))
        put(7, dv[:, HW:])
        put(6, dkf_ref[:, HW:].astype(F32) + dkb_ref[:, HW:].astype(F32))
        dp_ref[:, c0 + 4 * HW:c0 + 5 * HW] = dgo_ref[:, :HW]
        dp_ref[:, c0 + 8 * HW:c0 + 9 * HW] = dgo_ref[:, HW:]
        lr = lr_ref[...].astype(BF16)
        xg = _dot(lr, wgk_ref[...], NN) + bgk_ref[...]
        dxg = []
        for d, (raw_ref, dk_ref, dg_ref) in enumerate(((hf_ref, dkf_ref, dgf_ref), (hb_ref, dkb_ref, dgb_ref))):
            lbd = _hg_lb(lb_ref, d)
            s = _sig(raw_ref[...].astype(F32))
            f = lbd + (1.0 - lbd) * s
            df = dg_ref[:, :HW] / f - dk_ref[:, :HW].astype(F32)
            put(2 + d, df * (1.0 - lbd) * s * (1.0 - s))
            dlb_ref[d:d + 1, :] += _colsum(df * (1.0 - s)) * (lbd * (1.0 - lbd))
            dxg.append(dg_ref[:, HW:] * (1.0 / GLA_NORM) * _sig(-xg[:, d * HW:(d + 1) * HW]))
        dxg = jnp.concatenate(dxg, axis=1)
        db_ref[0:1, :] += _colsum(dxg)
        dxg_b = dxg.astype(BF16)
        dw_ref[...] += _dot(lr, dxg_b, TN)
        dlr = _dot(dxg_b, wgk_ref[...], NT)
        dp_ref[:, LR0:LR0 + DH] = (dlr + dgm_ref[:, :DH].astype(F32)).astype(BF16)
        dp_ref[:, LR0 + DH:GATE_GLA0] = dgm_ref[:, DH:D]
        dp_ref[:, GATE_GLA0:GATE_GLA0 + DH] = dgm_ref[:, D:GW] + dgm_ref[:, GW:GW + DH]
        dp_ref[:, GATE_GLA0 + DH:GATE_GLA0 + GW] = dgm_ref[:, GW + DH:]
        dp_ref[:, GATE_GLA0 + GW:] = jnp.zeros((TR, W_IN_COLS - GATE_GLA0 - GW), BF16)

    return pl.pallas_call(
        body, name="gates_bwd", grid=(t // TR,),
        in_specs=[seg(0), seg(2), seg(3), _rowcol(DH, LR0 // DH), _full((2, 2, HW)), _full((DH, D)), _full((1, D)),
                  _row(2 * GW), _row(D)] + [_row(D)] * 8,
        out_specs=[_row(W_IN_COLS), _full((8, HW)), _full((DH, D)), _full((8, D))],
        out_shape=[jax.ShapeDtypeStruct((t, W_IN_COLS), BF16), jax.ShapeDtypeStruct((8, HW), F32),
                   jax.ShapeDtypeStruct((DH, D), F32), jax.ShapeDtypeStruct((8, D), F32)],
        compiler_params=_cp("arbitrary"),
    )(p, p, p, p, hg_lb, wgk, bgk, dgm, dgo, dq_f, dq_b, dv_f, dv_b, dk_f, dk_b, dg_f, dg_b)


def _scan_consts(rev):
    r = lax.broadcasted_iota(jnp.int32, (CHUNK, CHUNK), 0)
    u = lax.broadcasted_iota(jnp.int32, (CHUNK, CHUNK), 1)
    rp = lax.broadcasted_iota(jnp.int32, (CHUNK, 1), 0)
    if rev:
        r, u, rp = CHUNK - 1 - r, CHUNK - 1 - u, CHUNK - 1 - rp
    tri = jnp.where(u <= r, 1.0, 0.0).astype(F32)
    tri_t = jnp.where(r <= u, 1.0, 0.0).astype(F32)
    lv = []
    for b in LEVELS:
        sh = b.bit_length() - 1
        pair = ((r >> sh) == (u >> sh) + 1) & (((u >> sh) & 1) == 0)
        pair_t = ((u >> sh) == (r >> sh) + 1) & (((r >> sh) & 1) == 0)
        tside = ((rp >> sh) & 1) == 1
        lv.append((pair, pair_t, tside, jnp.where(tside, 1.0, -1.0).astype(F32)))
    bd = LEVELS[-1].bit_length() - 1
    diag = ((r >> bd) == (u >> bd)) & (u <= r)
    diag_t = ((r >> bd) == (u >> bd)) & (r <= u)
    return tri, tri_t, lv, diag, diag_t


def _row_of(pos, rev):
    return CHUNK - 1 - pos if rev else pos


def _chunk_terms(cum, b_scr, consts, rev):
    _, _, lv, _, _ = consts
    terms = []
    for b, (_, _, _, sgn) in zip(LEVELS, lv):
        pieces = []
        for j in range(CHUNK // (2 * b)):
            row = _row_of(2 * b * j + b - 1, rev)
            pieces.append(jnp.broadcast_to(b_scr[row:row + 1, :], (2 * b, DH)))
        if rev:
            pieces = pieces[::-1]
        bnd = pieces[0] if len(pieces) == 1 else jnp.concatenate(pieces, axis=0)
        terms.append(jnp.exp((cum - bnd) * sgn))
    b = LEVELS[-1]
    pieces = []
    for j in range(CHUNK // b):
        if j == 0:
            pieces.append(jnp.zeros((b, DH), F32))
        else:
            row = _row_of(b * j - 1, rev)
            pieces.append(jnp.broadcast_to(b_scr[row:row + 1, :], (b, DH)))
    if rev:
        pieces = pieces[::-1]
    start = jnp.concatenate(pieces, axis=0)
    wq = jnp.exp(jnp.minimum(cum - start, 0.0))
    wk = jnp.exp(jnp.minimum(start - cum, EXP_CLAMP))
    terms.append((wq, wk))
    return terms


def _run_staged(units):
    live = list(units)
    while live:
        nxt = []
        for u in live:
            try:
                next(u)
                nxt.append(u)
            except StopIteration:
                pass
        live = nxt


SCAN_TB = 256
SCAN_CB = SCAN_TB // CHUNK


def _block_order(i, ntb, rev):
    nctx = CTX // SCAN_TB
    if not rev:
        return i
    return jnp.where(i < nctx, nctx - 1 - i, ntb - 1 - (i - nctx))


def _chunk_in_block(j, rev):
    return SCAN_CB - 1 - j if rev else j


def _scan_fwd(q, k, v, g, rev):
    t = q.shape[0]
    nc = t // CHUNK
    hpb = SCAN_HEADS_FWD

    def body(q_ref, k_ref, v_ref, g_ref, o_ref, st_ref, s_scr, b_scr):
        consts = _scan_consts(rev)
        _, _, lv, diag, _ = consts
        masks = [lvl[0] for lvl in lv] + [diag]

        @pl.when(pl.program_id(1) == 0)
        def _():
            s_scr[...] = jnp.zeros_like(s_scr)

        tri = consts[0]
        state = {hh: s_scr[hh] for hh in range(hpb)}

        def unit(hh, j):
            sl = slice(hh * DH, (hh + 1) * DH)
            c = _chunk_in_block(j, rev)
            rows = slice(c * CHUNK, (c + 1) * CHUNK)
            b_ref = b_scr.at[hh * SCAN_CB + j]
            qc, kc, vc, gc = q_ref[rows, sl], k_ref[rows, sl], v_ref[rows, sl], g_ref[rows, sl]
            cum = _split_dot(tri, gc)
            b_ref[...] = cum
            yield
            terms = _chunk_terms(cum, b_ref, consts, rev)
            qf, kf = qc.astype(F32), kc.astype(F32)
            xs = [(jnp.where(tside, qf, kf) * w).astype(BF16) for w, (_, _, tside, _) in zip(terms[:-1], lv)]
            qd, kd = (qf * terms[-1][0]).astype(BF16), (kf * terms[-1][1]).astype(BF16)
            tot = _colsum(gc)
            qe = (qf * jnp.exp(cum)).astype(BF16)
            ke = (kf * jnp.exp(tot - cum)).astype(BF16)
            vb = vc.astype(BF16)
            yield
            scs = [_dot(x, x, NT) for x in xs] + [_dot(qd, kd, NT)]
            kv = _dot(vb, ke, TN)
            yield
            a = jnp.zeros((CHUNK, CHUNK), F32)
            for sc, m in zip(scs, masks):
                a = a + jnp.where(m, sc, 0.0)
            o_intra = _dot(a.astype(BF16), vb, NN)
            yield
            st = state[hh]
            st_ref[hh, c] = st
            o_ref[rows, sl] = o_intra + _dot(qe, st.astype(BF16), NT)
            state[hh] = st * jnp.exp(tot) + kv
            yield

        _run_staged([unit(hh, j) for hh in range(hpb) for j in range(SCAN_CB)])
        for hh in range(hpb):
            s_scr[hh] = state[hh]

    ntb = t // SCAN_TB
    col = pl.BlockSpec((SCAN_TB, hpb * DH), lambda h, i: (_block_order(i, ntb, rev), h))
    return pl.pallas_call(
        body, name="scan_fwd_" + ("bw" if rev else "fw"), grid=(NH // hpb, ntb),
        in_specs=[col] * 4,
        out_specs=[col, pl.BlockSpec((hpb, SCAN_CB, DH, DH), lambda h, i: (h, _block_order(i, ntb, rev), 0, 0))],
        out_shape=[jax.ShapeDtypeStruct((t, D), F32), jax.ShapeDtypeStruct((NH, nc, DH, DH), F32)],
        scratch_shapes=[pltpu.VMEM((hpb, DH, DH), F32), pltpu.VMEM((hpb * SCAN_CB, CHUNK, DH), F32)],
        compiler_params=_cp("parallel", "arbitrary"),
    )(q, k, v, g)


def _scan_bwd(q, k, v, g, do, states, rev):
    t = q.shape[0]
    nc = t // CHUNK
    hpb = SCAN_HEADS_BWD

    def body(q_ref, k_ref, v_ref, g_ref, do_ref, st_ref, dq_ref, dk_ref, dv_ref, dg_ref, ds_scr, b_scr):
        consts = _scan_consts(rev)
        _, tri_t, lv, diag, diag_t = consts
        masks = [(lvl[0], lvl[1]) for lvl in lv] + [(diag, diag_t)]
        @pl.when(pl.program_id(1) == 0)
        def _():
            ds_scr[...] = jnp.zeros_like(ds_scr)

        tri = consts[0]
        dstate = {hh: ds_scr[hh] for hh in range(hpb)}

        def unit(hh, jj):
            sl = slice(hh * DH, (hh + 1) * DH)
            c = _chunk_in_block(SCAN_CB - 1 - jj, rev)
            rows = slice(c * CHUNK, (c + 1) * CHUNK)
            b_ref = b_scr.at[hh * SCAN_CB + jj]
            qc, kc, vc, gc = q_ref[rows, sl], k_ref[rows, sl], v_ref[rows, sl], g_ref[rows, sl]
            dob = do_ref[rows, sl].astype(BF16)
            vb = vc.astype(BF16)
            cum = _split_dot(tri, gc)
            b_ref[...] = cum
            da = _dot(dob, vb, NT)
            da_t = _dot(vb, dob, NT)
            yield
            terms = _chunk_terms(cum, b_ref, consts, rev)
            qf, kf = qc.astype(F32), kc.astype(F32)
            xs = [(jnp.where(tside, qf, kf) * w).astype(BF16) for w, (_, _, tside, _) in zip(terms[:-1], lv)]
            wqd, wkd = terms[-1]
            qdb, kdb = (qf * wqd).astype(BF16), (kf * wkd).astype(BF16)
            tot = _colsum(gc)
            e_tot = jnp.exp(tot)
            e_b = jnp.exp(cum)
            e_t = jnp.exp(tot - cum)
            qeb = (qf * e_b).astype(BF16)
            keb = (kf * e_t).astype(BF16)
            dsym = [(jnp.where(m, da, 0.0) + jnp.where(m_t, da_t, 0.0)).astype(BF16) for m, m_t in masks[:-1]]
            dad = (jnp.where(diag, da, 0.0).astype(BF16), jnp.where(diag_t, da_t, 0.0).astype(BF16))
            yield
            sym = [_dot(x, x, NT) for x in xs]
            dxs = [_dot(d, x, NN) for d, x in zip(dsym, xs)]
            at_d = _dot(kdb, qdb, NT)
            dqt_d = _dot(dad[0], kdb, NN)
            dkt_d = _dot(dad[1], qdb, NN)
            qd = _dot(dob, qeb, TN)
            yield
            a_t = jnp.where(diag_t, at_d, 0.0)
            dq = dqt_d * wqd
            dk = dkt_d * wkd
            db = dqt_d * qdb.astype(F32) - dkt_d * kdb.astype(F32)
            for s, dx, x, w, (_, m_t, tside, sgn) in zip(sym, dxs, xs, terms[:-1], lv):
                a_t = a_t + jnp.where(m_t, s, 0.0)
                dxw = dx * w
                dq = dq + jnp.where(tside, dxw, 0.0)
                dk = dk + jnp.where(tside, 0.0, dxw)
                db = db + (dx * x.astype(F32)) * sgn
            dv_intra = _dot(a_t.astype(BF16), dob, NN)
            st = st_ref[hh, c]
            stb = st.astype(BF16)
            dqe = _dot(dob, stb, NN)
            yield
            dst = dstate[hh]
            dstb = dst.astype(BF16)
            dstate[hh] = dst * e_tot + qd
            dv_ref[rows, sl] = (dv_intra + _dot(keb, dstb, NT)).astype(BF16)
            dke = _dot(vb, dstb, NN)
            yield
            qe = qeb.astype(F32)
            ke = keb.astype(F32)
            dq_ref[rows, sl] = (dq + dqe * e_b).astype(BF16)
            dk_ref[rows, sl] = (dk + dke * e_t).astype(BF16)
            db = db + dqe * qe - dke * ke
            dtot = _colsum(dstb.astype(F32) * stb.astype(F32)) * e_tot + _colsum(dke * ke)
            dg_ref[rows, sl] = _split_dot(tri_t, db) + dtot
            yield

        _run_staged([unit(hh, jj) for hh in range(hpb) for jj in range(SCAN_CB)])
        for hh in range(hpb):
            ds_scr[hh] = dstate[hh]

    ntb = t // SCAN_TB
    blk = lambda i: _block_order(ntb - 1 - i, ntb, rev)
    col = pl.BlockSpec((SCAN_TB, hpb * DH), lambda h, i: (blk(i), h))
    out = jax.ShapeDtypeStruct((t, D), F32)
    outb = jax.ShapeDtypeStruct((t, D), BF16)
    return pl.pallas_call(
        body, name="scan_bwd_" + ("bw" if rev else "fw"), grid=(NH // hpb, ntb),
        in_specs=[col] * 5 + [pl.BlockSpec((hpb, SCAN_CB, DH, DH), lambda h, i: (h, blk(i), 0, 0))],
        out_specs=[col] * 4,
        out_shape=[outb] * 3 + [out],
        scratch_shapes=[pltpu.VMEM((hpb, DH, DH), F32), pltpu.VMEM((hpb * SCAN_CB, CHUNK, DH), F32)],
        compiler_params=_cp("parallel", "arbitrary"),
    )(q, k, v, g, do, states)


W_IN_GRAD_CHUNKS = (("a", (0, 512)), ("b", (0, 256)), ("b", (256, 512)))
W_IN_REF = 6688


def _layout_w_in(w):
    return jnp.pad(w, ((0, 0), (0, W_IN_COLS - W_IN_REF)))


def _unlayout_w_in(d):
    return d[:, :W_IN_REF]


W_IN_PAD = 896


def _assemble_w_in(g):
    n, r, wp = g.shape
    tr = 256
    tiles = wp // DH

    def body(g_ref, o_ref):
        lane = lax.broadcasted_iota(jnp.int32, (tr, DH), 1)
        for t in range(W_IN_COLS // DH):
            acc = None
            for j in range(n):
                c = DH * t - W_IN_SHARD * j
                if c <= -DH or c >= W_IN_SHARD:
                    continue
                k, s = divmod(c, DH)
                lo = g_ref[j, :, k * DH:(k + 1) * DH] if 0 <= k < tiles else None
                hi = g_ref[j, :, (k + 1) * DH:(k + 2) * DH] if s and 0 <= k + 1 < tiles else None
                if s:
                    zero = jnp.zeros((tr, DH), g.dtype)
                    lo = zero if lo is None else pltpu.roll(lo, DH - s, 1)
                    hi = zero if hi is None else pltpu.roll(hi, DH - s, 1)
                    part = jnp.where(lane < DH - s, lo, hi)
                else:
                    part = lo
                acc = part if acc is None else acc + part
            o_ref[:, t * DH:(t + 1) * DH] = jnp.zeros((tr, DH), g.dtype) if acc is None else acc

    return pl.pallas_call(
        body, name="assemble_w_in", grid=(r // tr,),
        in_specs=[pl.BlockSpec((n, tr, wp), lambda i: (0, i, 0))],
        out_specs=pl.BlockSpec((tr, W_IN_COLS), lambda i: (i, 0)),
        out_shape=jax.ShapeDtypeStruct((r, W_IN_COLS), g.dtype),
        compiler_params=_cp("parallel"),
    )(g)


def _gate_cols(w):
    return jnp.pad(w, ((0, 0), (GOFF, GW - GOFF - D)))


def _gate_rows(w):
    return jnp.pad(w, ((GOFF, GW - GOFF - D), (0, 0)))


def _layout_wgk(w):
    r = w.shape[1]
    top = jnp.concatenate([w[0], jnp.zeros_like(w[0])], axis=1)
    bot = jnp.concatenate([jnp.zeros_like(w[1]), w[1]], axis=1)
    return jnp.concatenate([top, bot, jnp.zeros((DH - 2 * r, D), w.dtype)], axis=0)


def _unlayout_wgk(d, r=16):
    return jnp.stack([d[:r, :HW], d[r:2 * r, HW:]])


def _local_step(z, target, modc, modx, norms, onw, hg_lb, wgk, bgk, get_w_in, get_mix, get_ffn, send):
    n_pre1, n_post1, n_pre2, n_post2 = norms
    t = z[0].shape[0] + z[1].shape[0]
    tm = 1152 if t % 1152 == 0 else 256
    h1 = _prenorm(z, n_pre1, modc, modx, 0, 1, "prenorm1")
    w_in = get_w_in(h1)
    p = _matmul(h1, w_in, NN, BF16, "mm_in", t, 1024, D)
    q, v, k_f, k_b, g_f, g_b = _gates_fwd(p, hg_lb, wgk, bgk)
    o_f, st_f = _scan_fwd(q, k_f, v, g_f, False)
    o_b, st_b = _scan_fwd(q, k_b, v, g_b, True)
    y = _post_fwd(o_f, o_b, p, onw)
    w_br_hg, w_br_gla, w_out = get_mix(y)
    u1 = _matmul(y, w_br_hg, NN, BF16, "mm_br_hg", tm, GW, HW, a_off=0)
    u2 = _matmul(y, w_br_gla, NN, BF16, "mm_br_gla", tm, GW, HW, a_off=1)
    merged = _merge_fwd(p, u1, u2)
    y1 = _matmul(merged, w_out, NN, BF16, "mm_out", tm, 512, GW)
    z1, h2 = _mid_fwd(z, y1, n_post1, n_pre2, modc, modx)
    w_gu_t, w_down = get_ffn(h2)
    uv = _matmul(h2, w_gu_t, NT, BF16, "mm_gu", t, D_FF // 2, D)
    act = _swiglu_fwd(uv)
    y2 = _matmul(act, w_down, NN, BF16, "mm_down", t, 512, D_FF)
    dz, dy2, loss_vec, sm_final = _final(z1, y2, target, n_post2, modc, modx)
    dact = _matmul(dy2, w_down, NT, BF16, "mm_down_dx", t, D_FF // 2, D)
    d_w_down = _matmul(act, dy2, TN, BF16, "mm_down_dw", D_FF // 2, 1024, t)
    duv = _swiglu_bwd(uv, dact)
    dh2 = _matmul(duv, w_gu_t, NN, BF16, "mm_gu_dx", tm, 512, D_FF)
    d_w_gate_t = _matmul(duv, h2, TN, BF16, "mm_gate_dw", D_FF // 2, 1024, t, a_off=0, m_out=D_FF)
    d_w_up_t = _matmul(duv, h2, TN, BF16, "mm_up_dw", D_FF // 2, 1024, t, a_off=2, m_out=D_FF)
    dh2 = send(("w_down", "w_gate_t", "w_up_t"), (d_w_down, d_w_gate_t, d_w_up_t), dh2)
    dz, dy1, sm_mid = _mid_bwd(dh2, dz, z1, y1, n_post1, n_pre2, modc, modx)
    dmerged = _matmul(dy1, w_out, NT, BF16, "mm_out_dx", tm, GW, D)
    d_w_out = _matmul(merged, dy1, TN, BF16, "mm_out_dw", GW, 512, t)
    du1, du2, dgm = _merge_bwd(dmerged, p, u1, u2)
    dy_hg = _matmul(du1, w_br_hg, NT, BF16, "mm_br_hg_dx", tm, HW, GW)
    dy_gla = _matmul(du2, w_br_gla, NT, BF16, "mm_br_gla_dx", tm, HW, GW)
    d_w_br_hg = _matmul(y, du1, TN, BF16, "mm_br_hg_dw", HW, GW, t, a_off=0, m_out=HW)
    d_w_br_gla = _matmul(y, du2, TN, BF16, "mm_br_gla_dw", HW, GW, t, a_off=1, m_out=HW)
    dy_hg = send(("w_out", "w_br_hg", "w_br_gla"), (d_w_out, d_w_br_hg, d_w_br_gla), dy_hg)
    do, dgo, sm_post = _post_bwd(dy_hg, dy_gla, o_f, o_b, p, onw)
    dq_f, dk_f, dv_f, dg_f = _scan_bwd(q, k_f, v, g_f, do, st_f, False)
    dq_b, dk_b, dv_b, dg_b = _scan_bwd(q, k_b, v, g_b, do, st_b, True)
    dp, d_lb, d_wgk, d_bgk = _gates_bwd(p, hg_lb, wgk, bgk, dgm, dgo, dq_f, dq_b, dv_f, dv_b, dk_f, dk_b, dg_f, dg_b)
    d_w_in_a = _matmul(h1, dp, TN, BF16, "mm_in_dw_a", 512, 1024, t, a_off=0, m_out=D // 2)
    dp = send(("w_in_a",), (d_w_in_a,), dp)
    d_w_in_b = _matmul(h1, dp, TN, BF16, "mm_in_dw_b", 512, 1024, t, a_off=1, m_out=D // 2)
    dp = send(("w_in_b",), (d_w_in_b,), dp)
    dh1 = _matmul(dp, w_in, NT, BF16, "mm_in_dx", tm, 512, W_IN_COLS // 2)
    grad_x, sm_pre = _pre_bwd(dh1, dz, z, n_pre1, modc, modx)
    return dict(loss_vec=loss_vec, grad_x=grad_x, sm_final=sm_final, sm_mid=sm_mid, sm_post=sm_post, sm_pre=sm_pre,
                d_lb=d_lb, d_wgk=d_wgk, d_bgk=d_bgk)


MESH = pl.DeviceIdType.MESH
ANY = pl.BlockSpec(memory_space=pl.ANY)
N_REL = N_DEV - 1


def _place():
    return lax.axis_index("x"), lax.axis_index("y"), lax.axis_index("c")


def _slot(p):
    return 4 * p[0] + 2 * p[1] + p[2]


def _all_gather(arrays, name):
    n = len(arrays)

    def body(*refs):
        ins, outs = refs[:n], refs[n:2 * n]
        send_sems, recv_sems, local_sems = refs[2 * n:]
        x, y, c = _place()
        me, sibling = (x, y, c), (x, y, 1 - c)
        chips = [(1 - x, y), (x, 1 - y), (1 - x, 1 - y)]

        def copy(a, k, block, to, src=None):
            dst = outs[a].at[_slot(block)]
            return pltpu.make_async_remote_copy(
                src_ref=dst if src is None else src, dst_ref=dst,
                send_sem=send_sems.at[N_REL * a + k], recv_sem=recv_sems.at[N_REL * a + k],
                device_id=to, device_id_type=MESH)

        mine = [pltpu.make_async_copy(ins[a], outs[a].at[_slot(me)], local_sems.at[a]) for a in range(n)]
        for cp in mine:
            cp.start()
        first = []
        for a in range(n):
            first.append(copy(a, 0, me, sibling, src=ins[a]))
            first += [copy(a, 1 + j, me, (*chip, c), src=ins[a]) for j, chip in enumerate(chips)]
        for cp in first:
            cp.start()
        passed = []
        for j, chip in enumerate(chips):
            for a in range(n):
                copy(a, 1 + j, (*chip, c), me).wait_recv()
                fwd = copy(a, 4 + j, (*chip, c), sibling)
                fwd.start()
                passed.append(fwd)
        for a in range(n):
            copy(a, 0, sibling, me).wait_recv()
        for j, chip in enumerate(chips):
            for a in range(n):
                copy(a, 4 + j, (*chip, 1 - c), me).wait_recv()
        for cp in first + passed:
            cp.wait_send()
        for cp in mine:
            cp.wait()

    return pl.pallas_call(
        body, name=name,
        in_specs=[ANY] * n, out_specs=[ANY] * n,
        out_shape=[jax.ShapeDtypeStruct((N_DEV,) + a.shape, a.dtype) for a in arrays],
        scratch_shapes=[pltpu.SemaphoreType.DMA((N_REL * n,)), pltpu.SemaphoreType.DMA((N_REL * n,)),
                        pltpu.SemaphoreType.DMA((n,))],
    )(*arrays)


def _exchange(arrays, name):
    n = len(arrays)

    def body(*refs):
        ins, outs = refs[:n], refs[n:2 * n]
        send_sems, recv_sems, local_sems = refs[2 * n:]
        x, y, c = _place()
        me = _slot((x, y, c))
        mine = [pltpu.make_async_copy(ins[a].at[me], outs[a].at[me], local_sems.at[a]) for a in range(n)]
        for cp in mine:
            cp.start()
        copies = []
        for a in range(n):
            for k in range(1, N_DEV):
                flip = lambda v, bit: 1 - v if bit else v
                peer = (flip(x, k & 4), flip(y, k & 2), flip(c, k & 1))
                copies.append(pltpu.make_async_remote_copy(
                    src_ref=ins[a].at[_slot(peer)], dst_ref=outs[a].at[me],
                    send_sem=send_sems.at[N_REL * a + k - 1], recv_sem=recv_sems.at[N_REL * a + k - 1],
                    device_id=peer, device_id_type=MESH))
                copies[-1].start()
        i = 0
        for a in range(n):
            for k in range(1, N_DEV):
                flip = lambda v, bit: 1 - v if bit else v
                peer = (flip(x, k & 4), flip(y, k & 2), flip(c, k & 1))
                pltpu.make_async_remote_copy(
                    src_ref=ins[a].at[_slot(peer)], dst_ref=outs[a].at[_slot(peer)],
                    send_sem=send_sems.at[N_REL * a + k - 1], recv_sem=recv_sems.at[N_REL * a + k - 1],
                    device_id=peer, device_id_type=MESH).wait_recv()
                i += 1
        for cp in copies:
            cp.wait_send()
        for cp in mine:
            cp.wait()

    return pl.pallas_call(
        body, name=name,
        in_specs=[ANY] * n, out_specs=[ANY] * n,
        out_shape=[jax.ShapeDtypeStruct(a.shape, a.dtype) for a in arrays],
        scratch_shapes=[pltpu.SemaphoreType.DMA((N_REL * n,)), pltpu.SemaphoreType.DMA((N_REL * n,)),
                        pltpu.SemaphoreType.DMA((n,))],
    )(*arrays)


HBM = pl.BlockSpec(memory_space=pltpu.HBM)
SEM = pl.BlockSpec(memory_space=pltpu.SEMAPHORE)
EFFECT = pltpu.SideEffectType.DATAFLOW_SIDE_EFFECTING


def _peer_of(x, y, c, k):
    flip = lambda v, bit: 1 - v if bit else v
    return flip(x, k & 4), flip(y, k & 2), flip(c, k & 1)


def _view_whole(src, slot):
    return src


def _view_near(src, slot):
    return src


_view_near.peers = (1, 2, 4, 6)


def _view_block(src, slot):
    return src.at[slot]


W_IN_SHARD = W_IN_REF // N_DEV


def _view_window(rows):
    def view(src, slot):
        col0 = pl.multiple_of((W_IN_SHARD * slot // DH) * DH, DH)
        return src.at[pl.ds(rows[0], rows[1] - rows[0]), pl.ds(col0, D)]
    return view


def _split_copies(view, srcs, lands, send_sems, recv_sems, local_sems):
    x, y, c = _place()
    me = _slot((x, y, c))
    local, sends, waits = [], [], []
    for a, (src, land) in enumerate(zip(srcs, lands)):
        local.append(pltpu.make_async_copy(view(src, me), land.at[me], local_sems.at[a]))
        for k in getattr(view, "peers", range(1, N_DEV)):
            peer = _peer_of(x, y, c, k)
            mine = view(src, _slot(peer))
            sems = dict(send_sem=send_sems.at[N_REL * a + k - 1], recv_sem=recv_sems.at[N_REL * a + k - 1],
                        device_id=peer, device_id_type=MESH)
            sends.append(pltpu.make_async_remote_copy(src_ref=mine, dst_ref=land.at[me], **sems))
            waits.append(pltpu.make_async_remote_copy(src_ref=mine, dst_ref=land.at[_slot(peer)], **sems))
    return local, sends, waits


def _split_start(view, land_shapes, srcs, name, after):
    n = len(srcs)
    lands = [lax.empty(shp, s.dtype) for shp, s in zip(land_shapes, srcs)]

    def body(*refs):
        src_refs, land_refs = refs[:n], refs[n:2 * n]
        send_sems, recv_sems, local_sems = refs[2 * n + 1:2 * n + 4]
        token = refs[-1]
        local, sends, _ = _split_copies(view, src_refs, land_refs, send_sems, recv_sems, local_sems)
        for cp in local + sends:
            cp.start()
        token[...] = jnp.zeros_like(token)

    hbm = lambda a: pltpu.with_memory_space_constraint(a, pltpu.HBM)
    out = pl.pallas_call(
        body, name=name,
        out_shape=(pltpu.SemaphoreType.DMA((N_REL * n,)), pltpu.SemaphoreType.DMA((N_REL * n,)),
                   pltpu.SemaphoreType.DMA((n,)),
                   *[pltpu.HBM(s.shape, s.dtype) for s in srcs], *[pltpu.HBM(l.shape, l.dtype) for l in lands],
                   jax.ShapeDtypeStruct((8, DH), F32)),
        in_specs=[HBM] * (2 * n) + [ANY],
        out_specs=(SEM, SEM, SEM, *([HBM] * (2 * n)), pl.BlockSpec(memory_space=pltpu.VMEM)),
        input_output_aliases={i: 3 + i for i in range(2 * n)},
        compiler_params=pltpu.CompilerParams(has_side_effects=EFFECT),
    )(*[hbm(s) for s in srcs], *[hbm(l) for l in lands], after)
    handle = dict(view=view, n=n, sems=out[:3], srcs=list(out[3:3 + n]), lands=list(out[3 + n:3 + 2 * n]))
    return handle, out[-1]


def _split_wait(handle, name, after, srcs=None):
    view, n, sems, lands = handle["view"], handle["n"], handle["sems"], handle["lands"]
    srcs = handle["srcs"] if srcs is None else srcs
    afters = list(after) if isinstance(after, (list, tuple)) else [after]

    def body(*refs):
        src_refs, land_refs = refs[:n], refs[n:2 * n]
        send_sems, recv_sems, local_sems = refs[2 * n:2 * n + 3]
        local, _, waits = _split_copies(view, src_refs, land_refs, send_sems, recv_sems, local_sems)
        for cp in waits:
            cp.wait_send()
            cp.wait_recv()
        for cp in local:
            cp.wait()

    out = pl.pallas_call(
        body, name=name,
        out_shape=(*[pltpu.HBM(s.shape, s.dtype) for s in srcs], *[pltpu.HBM(l.shape, l.dtype) for l in lands]),
        in_specs=[HBM] * (2 * n) + [SEM, SEM, SEM] + [ANY] * len(afters),
        out_specs=tuple([HBM] * (2 * n)),
        input_output_aliases={i: i for i in range(2 * n)},
        compiler_params=pltpu.CompilerParams(has_side_effects=EFFECT),
    )(*srcs, *lands, *sems, *afters)
    handle["srcs"] = list(out[:n])
    return list(out[n:])


def _tie(x, token, name):
    def body(x_ref, t_ref, o_ref):
        pass

    return pl.pallas_call(
        body, name=name, out_shape=jax.ShapeDtypeStruct(x.shape, x.dtype),
        in_specs=[ANY, ANY], out_specs=ANY, input_output_aliases={0: 0},
    )(x, token)


def _forward_to_sibling(land, name):
    def body(land_ref, out_ref, send_sems, recv_sems):
        x, y, c = _place()
        sibling = (x, y, 1 - c)
        chips = [(1 - x, y), (x, 1 - y), (1 - x, 1 - y)]

        def copy(j, core):
            blk = _slot((*chips[j], core))
            return pltpu.make_async_remote_copy(src_ref=land_ref.at[blk], dst_ref=out_ref.at[blk],
                                                send_sem=send_sems.at[j], recv_sem=recv_sems.at[j],
                                                device_id=sibling, device_id_type=MESH)

        sends = [copy(j, c) for j in range(3)]
        for cp in sends:
            cp.start()
        for j in range(3):
            copy(j, 1 - c).wait_recv()
        for cp in sends:
            cp.wait_send()

    return pl.pallas_call(
        body, name=name, in_specs=[ANY], out_specs=ANY, input_output_aliases={0: 0},
        out_shape=jax.ShapeDtypeStruct(land.shape, land.dtype),
        scratch_shapes=[pltpu.SemaphoreType.DMA((3,)), pltpu.SemaphoreType.DMA((3,))],
    )(land)


def _mod_fwd(a, w, b):
    def body(a_ref, w_ref, b_ref, o_ref):
        o_ref[...] = _dot(_silu(a_ref[...]), w_ref[...], NN, precision=HI) + b_ref[...]

    return pl.pallas_call(
        body, name="mod_fwd", out_shape=jax.ShapeDtypeStruct((a.shape[0], w.shape[1]), F32),
        compiler_params=pltpu.CompilerParams(vmem_limit_bytes=VMEM_LIMIT),
    )(a, w, b)


def _mod_bwd(a, d, w):
    def body(a_ref, d_ref, w_ref, dw_ref, dc_ref):
        av = a_ref[...]
        dv = d_ref[...]
        dw_ref[...] = _dot(_silu(av), dv, TN, precision=HI)
        da = _dot(dv[0:8, :], w_ref[...], NT, precision=HI) * _dsilu(av[0:8, :])
        row = lax.broadcasted_iota(jnp.int32, da.shape, 0)
        dc_ref[...] = jnp.where(row == 0, da, 0.0)

    return pl.pallas_call(
        body, name="mod_bwd",
        out_shape=[jax.ShapeDtypeStruct(w.shape, F32), jax.ShapeDtypeStruct((8, w.shape[0]), F32)],
        compiler_params=pltpu.CompilerParams(vmem_limit_bytes=VMEM_LIMIT),
    )(a, d, w)


def _sum_devices(g):
    def body(g_ref, o_ref):
        acc = g_ref[0]
        for i in range(1, g.shape[0]):
            acc = acc + g_ref[i]
        o_ref[...] = acc

    return pl.pallas_call(body, name="sum_devices_%d" % g.shape[1],
                          out_shape=jax.ShapeDtypeStruct(g.shape[1:], F32))(g)


def _sum_windows(g, name):
    n, r, c = g.shape
    tr = 128

    def body(g_ref, o_ref):
        x, y, cc = _place()
        lane0 = (W_IN_SHARD * _slot((x, y, cc))) % DH
        acc = g_ref[0].astype(F32)
        for i in range(1, n):
            acc = acc + g_ref[i].astype(F32)
        o_ref[...] = pltpu.roll(acc, (c - lane0) % c, 1).T

    return pl.pallas_call(
        body, name=name, grid=(r // tr,),
        in_specs=[pl.BlockSpec((n, tr, c), lambda i: (0, i, 0))],
        out_specs=pl.BlockSpec((c, tr), lambda i: (0, i)),
        out_shape=jax.ShapeDtypeStruct((c, r), F32),
        compiler_params=_cp("parallel"),
    )(g)


def _adam_rows(r, c, n):
    budget = 6 * 1024 * 1024
    best = None
    for tr in range(16, r + 1, 16):
        if r % tr == 0 and tr * c * (2 * n + 28) <= budget:
            best = tr
    return best if best is not None else r


def _adamw(g, w, m, v, name):
    n, r, c = g.shape
    tr = _adam_rows(r, c, n)
    bc1 = 1.0 - ADAM_B1 ** ADAM_STEP
    bc2 = 1.0 - ADAM_B2 ** ADAM_STEP

    def body(g_ref, w_ref, m_ref, v_ref, go_ref, d_ref, mo_ref, vo_ref):
        grad = g_ref[0].astype(F32)
        for i in range(1, n):
            grad = grad + g_ref[i].astype(F32)
        go_ref[...] = grad
        m_new = ADAM_B1 * m_ref[...] + (1.0 - ADAM_B1) * grad
        v_new = ADAM_B2 * v_ref[...] + (1.0 - ADAM_B2) * (grad * grad)
        mo_ref[...] = m_new
        vo_ref[...] = v_new
        d_ref[...] = -ADAM_LR * ((m_new / bc1) / (jnp.sqrt(v_new / bc2) + ADAM_EPS) + ADAM_WD * w_ref[...])

    blk = pl.BlockSpec((tr, c), lambda i: (i, 0))
    out = jax.ShapeDtypeStruct((r, c), F32)
    return pl.pallas_call(
        body, name=name, grid=(r // tr,),
        in_specs=[pl.BlockSpec((n, tr, c), lambda i: (0, i, 0)), blk, blk, blk],
        out_specs=[blk] * 4, out_shape=[out] * 4,
        compiler_params=_cp("parallel"),
    )(g, w, m, v)


ADAM_ROWS3 = 168


def _adam_math(grad, w, m, v):
    bc1 = 1.0 - ADAM_B1 ** ADAM_STEP
    bc2 = 1.0 - ADAM_B2 ** ADAM_STEP
    m_new = ADAM_B1 * m + (1.0 - ADAM_B1) * grad
    v_new = ADAM_B2 * v + (1.0 - ADAM_B2) * (grad * grad)
    delta = -ADAM_LR * ((m_new / bc1) / (jnp.sqrt(v_new / bc2) + ADAM_EPS) + ADAM_WD * w)
    return delta, m_new, v_new


def _adamw_rows3(g, w3, m3, v3, name):
    r, _, c = w3.shape
    n = ADAM_ROWS3
    starts = list(range(0, r - n, n)) + [r - n]

    def body(g_hbm, w_hbm, m_hbm, v_hbm, go_hbm, d_hbm, mo_hbm, vo_hbm, gbuf, ibuf, obuf, in_sems, out_sems):
        def fetch(p):
            r0, slot = starts[p], p % 2
            g0 = (r0 // 8) * 8
            cps = [pltpu.make_async_copy(g_hbm.at[pl.ds(g0, n + 8)], gbuf.at[slot], in_sems.at[slot, 0])]
            cps += [pltpu.make_async_copy(h.at[pl.ds(r0, n), 0], ibuf.at[slot, k], in_sems.at[slot, 1 + k])
                    for k, h in enumerate((w_hbm, m_hbm, v_hbm))]
            for cp in cps:
                cp.start()
            return cps

        pending, outs = fetch(0), []
        for p, r0 in enumerate(starts):
            slot = p % 2
            nxt = fetch(p + 1) if p + 1 < len(starts) else []
            for cp in pending:
                cp.wait()
            grad = gbuf[slot, pl.ds(r0 - (r0 // 8) * 8, n), :]
            delta, m_new, v_new = _adam_math(grad, ibuf[slot, 0], ibuf[slot, 1], ibuf[slot, 2])
            for cp in outs:
                cp.wait()
            for k, val in enumerate((grad, delta, m_new, v_new)):
                obuf[slot, k] = val
            outs = [pltpu.make_async_copy(obuf.at[slot, k], h.at[pl.ds(r0, n), 0], out_sems.at[slot, k])
                    for k, h in enumerate((go_hbm, d_hbm, mo_hbm, vo_hbm))]
            for cp in outs:
                cp.start()
            pending = nxt
        for cp in outs:
            cp.wait()

    out = jax.ShapeDtypeStruct(w3.shape, F32)
    return pl.pallas_call(
        body, name=name, in_specs=[ANY] * 4, out_specs=[ANY] * 4, out_shape=[out] * 4,
        scratch_shapes=[pltpu.VMEM((2, n + 8, c), F32), pltpu.VMEM((2, 3, n, c), F32), pltpu.VMEM((2, 4, n, c), F32),
                        pltpu.SemaphoreType.DMA((2, 4)), pltpu.SemaphoreType.DMA((2, 4))],
        compiler_params=pltpu.CompilerParams(vmem_limit_bytes=VMEM_LIMIT),
    )(g, w3, m3, v3)


def kernel(x, c, ctx, c_ctx, w_mod, b_mod, norm_pre1, norm_post1, norm_pre2, norm_post2, w_in, hg_lb, hg_onorm, gla_w_gk, gla_b_gk, gla_onorm, w_br_hg, w_br_gla, w_out, w_ff_gate, w_ff_up, w_ff_down, loss_target, m_c_ctx, m_w_mod, m_b_mod, m_norm_pre1, m_norm_post1, m_norm_pre2, m_norm_post2, m_w_in, m_hg_lb, m_hg_onorm, m_gla_w_gk, m_gla_b_gk, m_gla_onorm, m_w_br_hg, m_w_br_gla, m_w_out, m_w_ff_gate, m_w_ff_up, m_w_ff_down, v_c_ctx, v_w_mod, v_b_mod, v_norm_pre1, v_norm_post1, v_norm_pre2, v_norm_post2, v_w_in, v_hg_lb, v_hg_onorm, v_gla_w_gk, v_gla_b_gk, v_gla_onorm, v_w_br_hg, v_w_br_gla, v_w_out, v_w_ff_gate, v_w_ff_up, v_w_ff_down):
    xi, yi, ci = lax.axis_index("x"), lax.axis_index("y"), lax.axis_index("c")
    me = 4 * xi + 2 * yi + ci
    t = CTX + x.shape[1]

    c_all, lb_g, wgk_g, bgk_g = _all_gather([c, hg_lb, gla_w_gk[0], gla_b_gk[0]], "ag_small")
    tr_ = lambda a: jnp.swapaxes(a[0], 0, 1)
    big = [w_in[0], w_br_hg[0], w_br_gla[0], w_out[0], tr_(w_ff_gate), tr_(w_ff_up), w_ff_down[0]]
    big_bf = [w.astype(BF16) for w in big]
    big_bf[0] = jnp.pad(big_bf[0], ((0, 0), (0, W_IN_PAD - W_IN_SHARD)))
    cols = lambda g: jnp.transpose(g, (1, 0, 2)).reshape(g.shape[1], N_DEV * g.shape[2])

    def get_w_in(after):
        land, = _split_wait(w_in_handle, "ag_w_in_wait", after)
        return _assemble_w_in(_forward_to_sibling(land, "ag_w_in_forward"))

    def get_mix(after):
        g_brh, g_brg, g_out = _split_wait(mix_handle, "ag_mix_wait", after)
        return _gate_cols(cols(g_brh)), _gate_cols(cols(g_brg)), _gate_rows(g_out.reshape(D, D))

    def get_ffn(after):
        g_gate, g_up, g_down = _split_wait(ffn_handle, "ag_ffn_wait", after)
        return (g_gate.reshape(D_FF, D), g_up.reshape(D_FF, D)), g_down.reshape(D_FF, D)

    hg_lb_full = jnp.transpose(lb_g, (1, 2, 0, 3)).reshape(2, 2, HW)
    wgk_k = _layout_wgk(jnp.transpose(wgk_g, (1, 2, 0, 3)).reshape(2, 16, HW)).astype(BF16)
    bgk_k = jnp.transpose(bgk_g, (1, 0, 2)).reshape(1, D)
    onw = jnp.concatenate([jnp.tile(hg_onorm, (1, NH // 2)), jnp.tile(gla_onorm, (1, NH // 2))], axis=1)

    n_mod = w_mod.shape[2]
    a9 = jnp.concatenate([c_ctx[None], c_all[:, 0], jnp.zeros((16 - 1 - N_DEV, D), F32)], axis=0)
    b_loc = lax.dynamic_slice(b_mod, (0, me * n_mod), (1, n_mod))
    s_loc = _mod_fwd(a9, w_mod[0], b_loc)
    s_all, = _all_gather([s_loc], "ag_mod")
    mod_all = jnp.transpose(s_all, (1, 0, 2)).reshape(16, N_DEV * n_mod)
    pad8 = lambda m: jnp.concatenate([m.reshape(6, D), jnp.zeros((2, D), F32)], axis=0)
    modc = pad8(mod_all[0])
    modx = pad8(lax.dynamic_slice(mod_all, (1 + me, 0), (1, N_DEV * n_mod))[0])

    gathered = lambda arrs: [(N_DEV,) + a.shape for a in arrs]
    w_in_handle, tok = _split_start(_view_near, gathered(big_bf[:1]), big_bf[:1], "ag_w_in_start", s_all)
    mix_handle, tok = _split_start(_view_whole, gathered(big_bf[1:4]), big_bf[1:4], "ag_mix_start", tok)
    ffn_handle, tok = _split_start(_view_whole, gathered(big_bf[4:]), big_bf[4:], "ag_ffn_start", tok)

    z = (ctx[0], x[0])
    modx = _tie(modx, tok, "tie_mod")
    norms = (norm_pre1, norm_post1, norm_pre2, norm_post2)
    shard = lambda d: jnp.transpose(d.reshape(d.shape[0], N_DEV, -1), (1, 0, 2)).astype(BF16)
    rowshard = lambda d: d.reshape(N_DEV, d.shape[0] // N_DEV, d.shape[1]).astype(BF16)
    sent, w_in_grad = [], {}

    def send_w_in(i, x_after):
        half, rows = W_IN_GRAD_CHUNKS[i]
        handle, tok = _split_start(_view_window(rows), [(N_DEV, rows[1] - rows[0], D)], w_in_grad[half],
                                   "grads_w_in%d_start" % i, x_after)
        w_in_grad[half] = handle["srcs"]
        sent.append(("w_in%d" % i, ["w_in#%d" % i], handle))
        return tok

    def send(names, grads, x_after):
        if names == ("w_in_a",):
            w_in_grad["a"] = list(grads)
            return _tie(x_after, send_w_in(0, x_after), "tie_w_in0")
        if names == ("w_in_b",):
            w_in_grad["b"] = list(grads)
            return x_after
        arrs, leaves = [], []
        for nm, g in zip(names, grads):
            if nm in ("w_gate_t", "w_up_t"):
                arrs.append(rowshard(g))
                leaves.append({"w_gate_t": "w_ff_gate", "w_up_t": "w_ff_up"}[nm])
            elif nm == "w_down":
                arrs.append(rowshard(g))
                leaves.append("w_ff_down")
            elif nm == "w_out":
                arrs.append(rowshard(g[GOFF:GOFF + D]))
                leaves.append(nm)
            else:
                arrs.append(shard(g[:, GOFF:GOFF + D]))
                leaves.append(nm)
        handle, tok = _split_start(_view_block, [a.shape for a in arrs], arrs, "grads_%s_start" % names[0], x_after)
        sent.append((names[0], leaves, handle))
        return _tie(x_after, tok, "tie_" + names[0])

    r = _local_step(z, loss_target[0], modc, modx, norms, onw, hg_lb_full, wgk_k, bgk_k,
                    get_w_in, get_mix, get_ffn, send)
    grad_x = r["grad_x"][None]

    sm_pre, sm_mid, sm_fin = r["sm_pre"], r["sm_mid"], r["sm_final"]
    dmodc = jnp.stack([sm_pre[0], sm_pre[2], sm_mid[4], sm_mid[0], sm_mid[2], sm_fin[0]]).reshape(-1)
    dmodx = jnp.stack([sm_pre[1], sm_pre[3], sm_mid[5], sm_mid[1], sm_mid[3], sm_fin[1]]).reshape(-1)
    on = r["sm_post"][0].reshape(NH, DH)
    pieces = [dmodc, dmodx, sm_pre[4], sm_mid[7], sm_mid[6], sm_fin[2], on[:NH // 2].sum(0), on[NH // 2:].sum(0),
              r["d_lb"][:2].reshape(-1), _unlayout_wgk(r["d_wgk"]).reshape(-1), r["d_bgk"][0]]
    loss_local = (0.5 / D) * jnp.sum(r["loss_vec"])
    pieces.append(jnp.concatenate([loss_local.reshape(1), jnp.zeros((DH - 1,), F32)]))
    sizes = [p.shape[0] for p in pieces]
    pack = jnp.concatenate(pieces).reshape(-1, DH)
    moms = [(m_w_in, v_w_in), (m_w_br_hg, v_w_br_hg), (m_w_br_gla, v_w_br_gla), (m_w_out, v_w_out),
            (m_w_ff_gate, v_w_ff_gate), (m_w_ff_up, v_w_ff_up), (m_w_ff_down, v_w_ff_down)]
    names = ["w_in", "w_br_hg", "w_br_gla", "w_out", "w_ff_gate", "w_ff_up", "w_ff_down"]
    wmv = {nm: (w, m, v) for nm, w, (m, v) in zip(names, big, moms)}
    res = {}

    def update(nm):
        w, m, v = wmv[nm]
        if nm in ("w_ff_gate", "w_ff_up"):
            outs = _adamw(recv[nm], w, tr_(m), tr_(v), "adamw_" + nm)
            res[nm] = [jnp.swapaxes(o, 0, 1)[None] for o in outs]
        else:
            res[nm] = [o[None] for o in _adamw(recv[nm], w, m[0], v[0], "adamw_" + nm)]

    small_handle, tok = _split_start(_view_whole, [(N_DEV,) + pack.shape], [pack], "small_grads_start", pack)
    tok = send_w_in(1, tok)
    recv = {}
    for first, leaves, handle in sent:
        if not first.startswith("w_in"):
            recv.update(zip(leaves, _split_wait(handle, "grads_%s_wait" % first, tok)))
    update("w_ff_gate")
    update("w_ff_up")
    pack_all, = _split_wait(small_handle, "small_grads_wait", [res["w_ff_gate"][0], res["w_ff_up"][0]])
    tot = _sum_devices(pack_all).reshape(-1)
    offs = [sum(sizes[:i]) for i in range(len(sizes))]
    part = lambda i: tot[offs[i]:offs[i] + sizes[i]]
    dmodc_t, dmodx_t = part(0), part(1)
    g_b_mod = (dmodc_t + dmodx_t)[None]
    g_norms = [part(i)[None] for i in (2, 3, 4, 5)]
    g_hg_on, g_gla_on = part(6)[None], part(7)[None]
    lb0 = lax.dynamic_slice(part(8).reshape(2, HW), (0, me * (HW // N_DEV)), (2, HW // N_DEV))
    g_hg_lb = jnp.stack([lb0, -lb0])
    g_wgk = lax.dynamic_slice(part(9).reshape(2, 16, HW), (0, 0, me * (HW // N_DEV)), (2, 16, HW // N_DEV))[None]
    g_bgk = lax.dynamic_slice(part(10).reshape(2, HW), (0, me * (HW // N_DEV)), (2, HW // N_DEV))[None]
    loss = part(11)[0]

    dmx_all = pack_all.reshape(N_DEV, -1)[:, sizes[0]:sizes[0] + sizes[1]]
    d9 = jnp.concatenate([lax.dynamic_slice(dmodc_t[None], (0, me * n_mod), (1, n_mod)),
                          lax.dynamic_slice(dmx_all, (0, me * n_mod), (N_DEV, n_mod)),
                          jnp.zeros((16 - 1 - N_DEV, n_mod), F32)], axis=0)
    g_w_mod, dcc_part = _mod_bwd(a9, d9, w_mod[0])
    cctx_handle, tok = _split_start(_view_whole, [(N_DEV,) + dcc_part.shape], [dcc_part], "c_ctx_start", dcc_part)
    tok = send_w_in(2, tok)
    recv["w_ff_down"] = _tie(recv["w_ff_down"], tok, "tie_down")
    update("w_ff_down")
    res["w_mod"] = [o[None] for o in _adamw(g_w_mod[None], w_mod[0], m_w_mod[0], v_w_mod[0], "adamw_w_mod")]
    for nm in ("w_out", "w_br_hg", "w_br_gla"):
        update(nm)
    dcc_all, = _split_wait(cctx_handle, "c_ctx_wait", [res["w_ff_down"][0], res["w_mod"][0]])
    g_c_ctx = _sum_devices(dcc_all)[0]

    small = [("c_ctx", c_ctx, m_c_ctx, v_c_ctx, g_c_ctx), ("b_mod", b_mod, m_b_mod, v_b_mod, g_b_mod),
             ("norm_pre1", norm_pre1, m_norm_pre1, v_norm_pre1, g_norms[0]),
             ("norm_post1", norm_post1, m_norm_post1, v_norm_post1, g_norms[1]),
             ("norm_pre2", norm_pre2, m_norm_pre2, v_norm_pre2, g_norms[2]),
             ("norm_post2", norm_post2, m_norm_post2, v_norm_post2, g_norms[3]),
             ("hg_lb", hg_lb, m_hg_lb, v_hg_lb, g_hg_lb), ("hg_onorm", hg_onorm, m_hg_onorm, v_hg_onorm, g_hg_on),
             ("gla_w_gk", gla_w_gk, m_gla_w_gk, v_gla_w_gk, g_wgk), ("gla_b_gk", gla_b_gk, m_gla_b_gk, v_gla_b_gk, g_bgk),
             ("gla_onorm", gla_onorm, m_gla_onorm, v_gla_onorm, g_gla_on)]
    flat = lambda k: jnp.concatenate([s[k].reshape(-1) for s in small]).reshape(-1, DH)
    outs = _adamw(flat(4)[None], flat(1), flat(2), flat(3), "adamw_small")
    off = 0
    for nm, w, _, _, _ in small:
        res[nm] = [o.reshape(-1)[off:off + w.size].reshape(w.shape) for o in outs]
        off += w.size

    done = [res[nm][0] for nm in names[1:]] + [res["w_mod"][0], outs[0]]
    sums = []
    for i, (first, leaves, handle) in enumerate(s for s in sent if s[0].startswith("w_in")):
        half = W_IN_GRAD_CHUNKS[i][0]
        land, = _split_wait(handle, "grads_%s_wait" % first, done, srcs=w_in_grad[half])
        w_in_grad[half] = handle["srcs"]
        sums.append(_sum_windows(land, "sum_windows%d" % i))
    major = lambda a: jnp.transpose(a, (2, 0, 1))
    outs = _adamw_rows3(jnp.concatenate(sums, axis=1), major(w_in), major(m_w_in), major(v_w_in), "adamw_w_in")
    res["w_in"] = [jnp.transpose(o, (1, 2, 0)) for o in outs]

    order = ["c_ctx", "w_mod", "b_mod", "norm_pre1", "norm_post1", "norm_pre2", "norm_post2", "w_in", "hg_lb",
             "hg_onorm", "gla_w_gk", "gla_b_gk", "gla_onorm", "w_br_hg", "w_br_gla", "w_out", "w_ff_gate", "w_ff_up",
             "w_ff_down"]
    return (loss, grad_x, *[res[n][k] for k in range(4) for n in order])
```

```python
import functools

import jax
import jax.numpy as jnp
from jax import lax
from jax.experimental import pallas as pl
from jax.experimental.pallas import tpu as pltpu

F32 = jnp.float32
BF16 = jnp.bfloat16
HI = lax.Precision.HIGHEST

N_DEV = 8
D = 1024
CTX = 256
HW = 512
DH = 128
NH = 8
D_FF = 2816
EPS = 1e-6
GLA_NORM = 16.0
CHUNK = 64
TR = 256
NCT = CTX // TR
W_IN_COLS = 7168
MAIN0 = 0
LR0 = 4608
GW = 1152
GOFF = 32
GATE_HG0 = LR0
GATE_GLA0 = LR0 + D
LEVELS = (32, 16, 8)
EXP_CLAMP = 80.0
VMEM_LIMIT = 48 * 1024 * 1024

ADAM_LR, ADAM_B1, ADAM_B2, ADAM_EPS, ADAM_WD, ADAM_STEP = 0.001, 0.9, 0.999, 1e-08, 0.01, 10


def _cp(*sem):
    return pltpu.CompilerParams(dimension_semantics=sem, vmem_limit_bytes=VMEM_LIMIT)


def _sig(x):
    return jax.nn.sigmoid(x)


def _silu(x):
    return x * _sig(x)


def _dsilu(x):
    s = _sig(x)
    return s * (1.0 + x * (1.0 - s))


def _rstd(x):
    return lax.rsqrt(jnp.mean(x * x, axis=-1, keepdims=True) + EPS)


def _rms_bwd(a, y, r):
    return r * (a - y * (r * r) * jnp.mean(a * y, axis=-1, keepdims=True))


def _colsum(x):
    return jnp.sum(x, axis=0, keepdims=True)


def _dot(a, b, dims, precision=None):
    return lax.dot_general(a, b, (dims, ((), ())), preferred_element_type=F32, precision=precision)


NN = ((1,), (0,))
NT = ((1,), (1,))
TN = ((0,), (0,))

SCAN_HEADS_FWD = 4
SCAN_HEADS_BWD = 4


def _split_dot(m, x):
    mb = m.astype(BF16)
    x1 = x.astype(BF16)
    r1 = x - x1.astype(F32)
    x2 = r1.astype(BF16)
    x3 = (r1 - x2.astype(F32)).astype(BF16)
    return _dot(mb, x1, NN) + _dot(mb, x2, NN) + _dot(mb, x3, NN)


def _matmul(a, b, dims, out_dtype, name, tm, tn, tk, a_off=0, m_out=None, a_roff=0, b_coff=0, n_out=None,
            out_t=False):
    pair = isinstance(b, (tuple, list))
    bs = list(b) if pair else [b]
    b1 = bs[0]
    rows = b1.shape[0] * len(bs)
    half = None
    if dims == NN:
        m, k, n = (a.shape[0] if m_out is None else m_out), rows, (b1.shape[1] if n_out is None else n_out)
        a_spec = pl.BlockSpec((tm, tk), lambda i, j, kk: (i + a_roff, kk + a_off))
        half = b1.shape[0] // tk
        b_maps = [lambda i, j, kk: (kk, j + b_coff)] if not pair else [
            lambda i, j, kk: (jnp.minimum(kk, half - 1), j), lambda i, j, kk: (jnp.maximum(kk - half, 0), j)]
        b_specs = [pl.BlockSpec((tk, tn), f) for f in b_maps]
        axis = 2
    elif dims == NT:
        m, k, n = a.shape[0], b1.shape[1], rows
        a_spec = pl.BlockSpec((tm, tk), lambda i, j, kk: (i, kk + a_off))
        half = b1.shape[0] // tn
        b_maps = [lambda i, j, kk: (j, kk)] if not pair else [
            lambda i, j, kk: (jnp.minimum(j, half - 1), kk), lambda i, j, kk: (jnp.maximum(j - half, 0), kk)]
        b_specs = [pl.BlockSpec((tn, tk), f) for f in b_maps]
        axis = 1
    else:
        assert not pair
        m, k = (a.shape[1] if m_out is None else m_out), a.shape[0]
        n = b1.shape[1]
        a_spec = pl.BlockSpec((tk, tm), lambda i, j, kk: (kk, i + a_off))
        b_specs = [pl.BlockSpec((tk, tn), lambda i, j, kk: (kk, j))]
    assert m % tm == 0 and n % tn == 0 and k % tk == 0, (name, m, n, k, tm, tn, tk)
    nk = k // tk
    assert not out_t or (dims == NN and nk == 1)
    nb = len(bs)

    def body(a_ref, *refs):
        o_ref = refs[nb]
        if pair:
            bv = jnp.where(pl.program_id(axis) < half, refs[0][...], refs[1][...])
        else:
            bv = refs[0][...]
        part = _dot(a_ref[...], bv, dims)
        if nk == 1:
            o_ref[...] = (part.T if out_t else part).astype(o_ref.dtype)
            return
        acc_ref = refs[nb + 1]
        kk = pl.program_id(2)

        @pl.when(kk == 0)
        def _():
            acc_ref[...] = part

        @pl.when(kk > 0)
        def _():
            acc_ref[...] += part

        @pl.when(kk == nk - 1)
        def _():
            o_ref[...] = acc_ref[...].astype(o_ref.dtype)

    return pl.pallas_call(
        body,
        name=name,
        grid=(m // tm, n // tn, nk),
        in_specs=[a_spec] + b_specs,
        out_specs=(pl.BlockSpec((tn, tm), lambda i, j, kk: (j, i)) if out_t else
                   pl.BlockSpec((tm, tn), lambda i, j, kk: (i, j))),
        out_shape=jax.ShapeDtypeStruct((n, m) if out_t else (m, n), out_dtype),
        scratch_shapes=[] if nk == 1 else [pltpu.VMEM((tm, tn), F32)],
        compiler_params=_cp("parallel", "parallel", "arbitrary"),
    )(a, *bs)


def _transpose(x, name):
    t, c = x.shape

    def body(x_ref, o_ref):
        o_ref[...] = x_ref[...].T

    return pl.pallas_call(
        body, name=name, grid=(t // TR,),
        in_specs=[pl.BlockSpec((TR, c), lambda i: (i, 0))],
        out_specs=pl.BlockSpec((c, TR), lambda i: (0, i)),
        out_shape=jax.ShapeDtypeStruct((c, t), x.dtype),
        compiler_params=_cp("parallel"),
    )(x)


def _row(c):
    return pl.BlockSpec((TR, c), lambda i: (i, 0))


def _rowcol(width, cb):
    return pl.BlockSpec((TR, width), lambda i: (i, cb))


def _full(shape):
    return pl.BlockSpec(shape, lambda i: (0,) * len(shape))


def _mod_row(mc_ref, mx_ref, k, is_ctx):
    return jnp.where(is_ctx, mc_ref[k:k + 1, :], mx_ref[k:k + 1, :])


def _z_specs():
    return [pl.BlockSpec((TR, D), lambda i: (jnp.minimum(i, NCT - 1), 0)),
            pl.BlockSpec((TR, D), lambda i: (jnp.maximum(i - NCT, 0), 0))]


def _z_tile(c_ref, x_ref, is_ctx):
    return jnp.where(is_ctx, c_ref[...], x_ref[...])


def _acc_row(ref, k, val):
    ref[k:k + 1, :] += val


def _acc_mod(ref, k, is_ctx, val):
    zero = jnp.zeros_like(val)
    ref[k:k + 1, :] += jnp.where(is_ctx, val, zero)
    ref[k + 1:k + 2, :] += jnp.where(is_ctx, zero, val)


def _prenorm(z, nw, modc, modx, i_shift, i_scale, name):
    t = z[0].shape[0] + z[1].shape[0]

    def body(zc_ref, zx_ref, nw_ref, mc_ref, mx_ref, h_ref):
        is_ctx = pl.program_id(0) < NCT
        x = _z_tile(zc_ref, zx_ref, is_ctx)
        n = x * _rstd(x) * nw_ref[...]
        h = n * (1.0 + _mod_row(mc_ref, mx_ref, i_scale, is_ctx)) + _mod_row(mc_ref, mx_ref, i_shift, is_ctx)
        h_ref[...] = h.astype(BF16)

    return pl.pallas_call(
        body, name=name, grid=(t // TR,),
        in_specs=_z_specs() + [_full((1, D)), _full((8, D)), _full((8, D))],
        out_specs=_row(D),
        out_shape=jax.ShapeDtypeStruct((t, D), BF16),
        compiler_params=_cp("parallel"),
    )(*z, nw, modc, modx)


def _hg_lb(lb_ref, d):
    a0 = lb_ref[0, d:d + 1, :]
    a1 = lb_ref[1, d:d + 1, :]
    mx = jnp.maximum(a0, a1)
    e0 = jnp.exp(a0 - mx)
    e1 = jnp.exp(a1 - mx)
    return e0 / (e0 + e1)


def _log_sigmoid(x):
    return jnp.minimum(x, 0.0) - jnp.log(1.0 + jnp.exp(-jnp.abs(x)))


def _gates_fwd(p, hg_lb, wgk, bgk):
    t = p.shape[0]
    seg = lambda j: _rowcol(HW, MAIN0 // HW + j)

    def body(hq_ref, hi_ref, hf_ref, hb_ref, gq_ref, gk_ref, gv_ref, lr_ref, lb_ref, wgk_ref, bgk_ref,
             q_ref, v_ref, kf_ref, kb_ref, gf_ref, gb_ref):
        q_ref[:, :HW] = _silu(hq_ref[...].astype(F32)).astype(BF16)
        q_ref[:, HW:] = (gq_ref[...].astype(F32) * (DH ** -0.5)).astype(BF16)
        v_ref[:, :HW] = hi_ref[...]
        v_ref[:, HW:] = gv_ref[...]
        xg = _dot(lr_ref[...].astype(BF16), wgk_ref[...], NN) + bgk_ref[...]
        for d, (raw_ref, k_ref, g_ref) in enumerate(((hf_ref, kf_ref, gf_ref), (hb_ref, kb_ref, gb_ref))):
            lbd = _hg_lb(lb_ref, d)
            f = lbd + (1.0 - lbd) * _sig(raw_ref[...].astype(F32))
            k_ref[:, :HW] = (1.0 - f).astype(BF16)
            k_ref[:, HW:] = gk_ref[...]
            g_ref[:, :HW] = jnp.log(f)
            g_ref[:, HW:] = _log_sigmoid(xg[:, d * HW:(d + 1) * HW]) * (1.0 / GLA_NORM)

    out = jax.ShapeDtypeStruct((t, D), F32)
    outb = jax.ShapeDtypeStruct((t, D), BF16)
    return pl.pallas_call(
        body, name="gates_fwd", grid=(t // TR,),
        in_specs=[seg(0), seg(1), seg(2), seg(3), seg(5), seg(6), seg(7), _rowcol(DH, LR0 // DH),
                  _full((2, 2, HW)), _full((DH, D)), _full((1, D))],
        out_specs=[_row(D)] * 6,
        out_shape=[outb] * 4 + [out] * 2,
        compiler_params=_cp("parallel"),
    )(p, p, p, p, p, p, p, p, hg_lb, wgk, bgk)


def _post_fwd(o_fw, o_bw, p, onw):
    t = o_fw.shape[0]

    def body(of_ref, ob_ref, g1_ref, g2_ref, w_ref, y_ref):
        for h in range(NH):
            sl = slice(h * DH, (h + 1) * DH)
            o = of_ref[:, sl] + ob_ref[:, sl]
            g_ref = g1_ref if h < NH // 2 else g2_ref
            gs = slice((h % (NH // 2)) * DH, (h % (NH // 2) + 1) * DH)
            n = o * _rstd(o) * w_ref[:, sl]
            y_ref[:, sl] = (n * _silu(g_ref[:, gs].astype(F32))).astype(BF16)

    return pl.pallas_call(
        body, name="post_fwd", grid=(t // TR,),
        in_specs=[_row(D), _row(D), _rowcol(HW, MAIN0 // HW + 4), _rowcol(HW, MAIN0 // HW + 8), _full((1, D))],
        out_specs=_row(D),
        out_shape=jax.ShapeDtypeStruct((t, D), BF16),
        compiler_params=_cp("parallel"),
    )(o_fw, o_bw, p, p, onw)


def _gate_window_specs(col0):
    return [_rowcol(HW, col0 // HW), _rowcol(HW, col0 // HW + 1), _rowcol(DH, (col0 + 2 * HW) // DH)]


def _gate_window(refs):
    return jnp.concatenate([r[...].astype(F32) for r in refs], axis=1)


def _merge_fwd(p, u1, u2):
    t = p.shape[0]

    def body(a0, a1, a2, b0, b1, b2, u1_ref, u2_ref, m_ref):
        f = lambda r: r[...].astype(F32)
        m_ref[...] = (_sig(_gate_window((a0, a1, a2))) * f(u1_ref)
                      + _sig(_gate_window((b0, b1, b2))) * f(u2_ref)).astype(BF16)

    return pl.pallas_call(
        body, name="merge_fwd", grid=(t // TR,),
        in_specs=_gate_window_specs(GATE_HG0) + _gate_window_specs(GATE_GLA0) + [_row(GW), _row(GW)],
        out_specs=_row(GW),
        out_shape=jax.ShapeDtypeStruct((t, GW), BF16),
        compiler_params=_cp("parallel"),
    )(p, p, p, p, p, p, u1, u2)


def _mid_fwd(z, y1, nw_post, nw_pre, modc, modx):
    t = y1.shape[0]

    def body(zc_ref, zx_ref, y_ref, wpo_ref, wpr_ref, mc_ref, mx_ref, z1_ref, h_ref):
        is_ctx = pl.program_id(0) < NCT
        y = y_ref[...].astype(F32)
        z1 = _z_tile(zc_ref, zx_ref, is_ctx) + _mod_row(mc_ref, mx_ref, 2, is_ctx) * (y * _rstd(y) * wpo_ref[...])
        z1_ref[...] = z1
        n = z1 * _rstd(z1) * wpr_ref[...]
        h = n * (1.0 + _mod_row(mc_ref, mx_ref, 4, is_ctx)) + _mod_row(mc_ref, mx_ref, 3, is_ctx)
        h_ref[...] = h.astype(BF16)

    return pl.pallas_call(
        body, name="mid_fwd", grid=(t // TR,),
        in_specs=_z_specs() + [_row(D), _full((1, D)), _full((1, D)), _full((8, D)), _full((8, D))],
        out_specs=[_row(D), _row(D)],
        out_shape=[jax.ShapeDtypeStruct((t, D), F32), jax.ShapeDtypeStruct((t, D), BF16)],
        compiler_params=_cp("parallel"),
    )(*z, y1, nw_post, nw_pre, modc, modx)


def _swiglu_fwd(uv):
    t = uv.shape[0]

    def body(u_ref, v_ref, a_ref):
        a_ref[...] = (_silu(u_ref[...].astype(F32)) * v_ref[...].astype(F32)).astype(BF16)

    return pl.pallas_call(
        body, name="swiglu_fwd", grid=(t // TR,),
        in_specs=[_rowcol(D_FF, 0), _rowcol(D_FF, 1)],
        out_specs=_row(D_FF),
        out_shape=jax.ShapeDtypeStruct((t, D_FF), BF16),
        compiler_params=_cp("parallel"),
    )(uv, uv)


def _swiglu_bwd(uv, da):
    t = uv.shape[0]

    def body(u_ref, v_ref, da_ref, d_ref):
        u = u_ref[...].astype(F32)
        d = da_ref[...].astype(F32)
        d_ref[:, :D_FF] = (d * v_ref[...].astype(F32) * _dsilu(u)).astype(BF16)
        d_ref[:, D_FF:] = (d * _silu(u)).astype(BF16)

    return pl.pallas_call(
        body, name="swiglu_bwd", grid=(t // TR,),
        in_specs=[_rowcol(D_FF, 0), _rowcol(D_FF, 1), _row(D_FF)],
        out_specs=_row(2 * D_FF),
        out_shape=jax.ShapeDtypeStruct((t, 2 * D_FF), BF16),
        compiler_params=_cp("parallel"),
    )(uv, uv, da)


def _final(z1, y2, target, nw, modc, modx):
    t = z1.shape[0]

    def body(z1_ref, y_ref, tg_ref, w_ref, mc_ref, mx_ref, dz_ref, dy_ref, loss_ref, sm_ref):
        i = pl.program_id(0)
        is_ctx = i < NCT

        @pl.when(i == 0)
        def _():
            loss_ref[...] = jnp.zeros_like(loss_ref)
            sm_ref[...] = jnp.zeros_like(sm_ref)

        g = _mod_row(mc_ref, mx_ref, 5, is_ctx)
        y = y_ref[...].astype(F32)
        r = _rstd(y)
        w = w_ref[...]
        yr = y * r
        n = yr * w
        e = z1_ref[...] + g * n - tg_ref[...]
        lat = jnp.where(is_ctx, 0.0, 1.0)
        loss_ref[...] += lat * _colsum(e * e)
        dz = e * (lat / D)
        dz_ref[...] = dz
        _acc_mod(sm_ref, 0, is_ctx, _colsum(dz * n))
        dn = dz * g
        _acc_row(sm_ref, 2, _colsum(dn * yr))
        dy_ref[...] = _rms_bwd(dn * w, y, r).astype(BF16)

    return pl.pallas_call(
        body, name="final", grid=(t // TR,),
        in_specs=[_row(D), _row(D), pl.BlockSpec((TR, D), lambda i: (jnp.maximum(i - NCT, 0), 0)),
                  _full((1, D)), _full((8, D)), _full((8, D))],
        out_specs=[_row(D), _row(D), _full((1, D)), _full((8, D))],
        out_shape=[jax.ShapeDtypeStruct((t, D), F32), jax.ShapeDtypeStruct((t, D), BF16),
                   jax.ShapeDtypeStruct((1, D), F32), jax.ShapeDtypeStruct((8, D), F32)],
        compiler_params=_cp("arbitrary"),
    )(z1, y2, target, nw, modc, modx)


def _mid_bwd(dh2, dz, z1, y1, nw_post, nw_pre, modc, modx):
    t = z1.shape[0]

    def body(dh_ref, dz_ref, z1_ref, y_ref, wpo_ref, wpr_ref, mc_ref, mx_ref, dzo_ref, dy_ref, sm_ref):
        i = pl.program_id(0)
        is_ctx = i < NCT

        @pl.when(i == 0)
        def _():
            sm_ref[...] = jnp.zeros_like(sm_ref)

        dh = dh_ref[...].astype(F32)
        z1 = z1_ref[...]
        r = _rstd(z1)
        zr = z1 * r
        wpr = wpr_ref[...]
        n = zr * wpr
        _acc_mod(sm_ref, 0, is_ctx, _colsum(dh))
        _acc_mod(sm_ref, 2, is_ctx, _colsum(dh * n))
        dn = dh * (1.0 + _mod_row(mc_ref, mx_ref, 4, is_ctx))
        _acc_row(sm_ref, 6, _colsum(dn * zr))
        dz1 = dz_ref[...] + _rms_bwd(dn * wpr, z1, r)
        dzo_ref[...] = dz1
        y = y_ref[...].astype(F32)
        r1 = _rstd(y)
        yr = y * r1
        wpo = wpo_ref[...]
        g = _mod_row(mc_ref, mx_ref, 2, is_ctx)
        _acc_mod(sm_ref, 4, is_ctx, _colsum(dz1 * (yr * wpo)))
        dn1 = dz1 * g
        _acc_row(sm_ref, 7, _colsum(dn1 * yr))
        dy_ref[...] = _rms_bwd(dn1 * wpo, y, r1).astype(BF16)

    return pl.pallas_call(
        body, name="mid_bwd", grid=(t // TR,),
        in_specs=[_row(D)] * 4 + [_full((1, D)), _full((1, D)), _full((8, D)), _full((8, D))],
        out_specs=[_row(D), _row(D), _full((8, D))],
        out_shape=[jax.ShapeDtypeStruct((t, D), F32), jax.ShapeDtypeStruct((t, D), BF16),
                   jax.ShapeDtypeStruct((8, D), F32)],
        compiler_params=_cp("arbitrary"),
    )(dh2, dz, z1, y1, nw_post, nw_pre, modc, modx)


def _pre_bwd(dh1, dz, z, nw, modc, modx):
    t = dh1.shape[0]

    def body(dh_ref, dz_ref, zc_ref, zx_ref, w_ref, mc_ref, mx_ref, dzo_ref, sm_ref):
        i = pl.program_id(0)
        is_ctx = i < NCT

        @pl.when(i == 0)
        def _():
            sm_ref[...] = jnp.zeros_like(sm_ref)

        dh = dh_ref[...].astype(F32)
        x = _z_tile(zc_ref, zx_ref, is_ctx)
        r = _rstd(x)
        xr = x * r
        w = w_ref[...]
        _acc_mod(sm_ref, 0, is_ctx, _colsum(dh))
        _acc_mod(sm_ref, 2, is_ctx, _colsum(dh * (xr * w)))
        dn = dh * (1.0 + _mod_row(mc_ref, mx_ref, 1, is_ctx))
        _acc_row(sm_ref, 4, _colsum(dn * xr))
        dzo_ref[...] = dz_ref[...] + _rms_bwd(dn * w, x, r)

    return pl.pallas_call(
        body, name="pre_bwd", grid=(t // TR,),
        in_specs=[_row(D)] * 2 + _z_specs() + [_full((1, D)), _full((8, D)), _full((8, D))],
        out_specs=[pl.BlockSpec((TR, D), lambda i: (jnp.maximum(i - NCT, 0), 0)), _full((8, D))],
        out_shape=[jax.ShapeDtypeStruct((t - CTX, D), F32), jax.ShapeDtypeStruct((8, D), F32)],
        compiler_params=_cp("arbitrary"),
    )(dh1, dz, *z, nw, modc, modx)


def _merge_bwd(dm, p, u1, u2):
    t = dm.shape[0]

    def body(dm_ref, a0, a1, a2, b0, b1, b2, u1_ref, u2_ref, du1_ref, du2_ref, dg_ref):
        dm_ = dm_ref[...].astype(F32)
        s1 = _sig(_gate_window((a0, a1, a2)))
        s2 = _sig(_gate_window((b0, b1, b2)))
        du1_ref[...] = (dm_ * s1).astype(BF16)
        du2_ref[...] = (dm_ * s2).astype(BF16)
        dg_ref[:, :GW] = (dm_ * u1_ref[...].astype(F32) * s1 * (1.0 - s1)).astype(BF16)
        dg_ref[:, GW:] = (dm_ * u2_ref[...].astype(F32) * s2 * (1.0 - s2)).astype(BF16)

    return pl.pallas_call(
        body, name="merge_bwd", grid=(t // TR,),
        in_specs=[_row(GW)] + _gate_window_specs(GATE_HG0) + _gate_window_specs(GATE_GLA0) + [_row(GW), _row(GW)],
        out_specs=[_row(GW), _row(GW), _row(2 * GW)],
        out_shape=[jax.ShapeDtypeStruct((t, GW), BF16), jax.ShapeDtypeStruct((t, GW), BF16),
                   jax.ShapeDtypeStruct((t, 2 * GW), BF16)],
        compiler_params=_cp("parallel"),
    )(dm, p, p, p, p, p, p, u1, u2)


def _post_bwd(dy_hg, dy_gla, o_fw, o_bw, p, onw):
    t = o_fw.shape[0]

    def body(d1_ref, d2_ref, of_ref, ob_ref, g1_ref, g2_ref, w_ref, do_ref, dg_ref, sm_ref):
        @pl.when(pl.program_id(0) == 0)
        def _():
            sm_ref[...] = jnp.zeros_like(sm_ref)

        for h in range(NH):
            sl = slice(h * DH, (h + 1) * DH)
            gs = slice((h % (NH // 2)) * DH, (h % (NH // 2) + 1) * DH)
            g_ref, d_ref = (g1_ref, d1_ref) if h < NH // 2 else (g2_ref, d2_ref)
            o = of_ref[:, sl] + ob_ref[:, sl]
            r = _rstd(o)
            orr = o * r
            w = w_ref[:, sl]
            gt = g_ref[:, gs].astype(F32)
            dy = d_ref[:, gs].astype(F32)
            dg_ref[:, sl] = (dy * (orr * w) * _dsilu(gt)).astype(BF16)
            dn = dy * _silu(gt)
            sm_ref[0:1, sl] += _colsum(dn * orr)
            do_ref[:, sl] = _rms_bwd(dn * w, o, r)

    return pl.pallas_call(
        body, name="post_bwd", grid=(t // TR,),
        in_specs=[_row(HW), _row(HW), _row(D), _row(D), _rowcol(HW, MAIN0 // HW + 4), _rowcol(HW, MAIN0 // HW + 8),
                  _full((1, D))],
        out_specs=[_row(D), _row(D), _full((8, D))],
        out_shape=[jax.ShapeDtypeStruct((t, D), F32), jax.ShapeDtypeStruct((t, D), BF16),
                   jax.ShapeDtypeStruct((8, D), F32)],
        compiler_params=_cp("arbitrary"),
    )(dy_hg, dy_gla, o_fw, o_bw, p, p, onw)


def _gates_bwd(p, hg_lb, wgk, bgk, dgm, dgo, dq_f, dq_b, dv_f, dv_b, dk_f, dk_b, dg_f, dg_b):
    t = p.shape[0]
    seg = lambda j: _rowcol(HW, MAIN0 // HW + j)

    def body(hq_ref, hf_ref, hb_ref, lr_ref, lb_ref, wgk_ref, bgk_ref, dgm_ref, dgo_ref,
             dqf_ref, dqb_ref, dvf_ref, dvb_ref, dkf_ref, dkb_ref, dgf_ref, dgb_ref,
             dp_ref, dlb_ref, dw_ref, db_ref):
        @pl.when(pl.program_id(0) == 0)
        def _():
            dlb_ref[...] = jnp.zeros_like(dlb_ref)
            dw_ref[...] = jnp.zeros_like(dw_ref)
            db_ref[...] = jnp.zeros_like(db_ref)

        c0 = MAIN0

        def put(j, val):
            dp_ref[:, c0 + j * HW:c0 + (j + 1) * HW] = val.astype(BF16)

        dq = dqf_ref[...].astype(F32) + dqb_ref[...].astype(F32)
        dv = dvf_ref[...].astype(F32) + dvb_ref[...].astype(F32)
        put(0, dq[:, :HW] * _dsilu(hq_ref[...].astype(F32)))
        put(1, dv[:, :HW])
        put(5, dq[:, HW:] * (DH ** -0.5))
        put(7, dv[:, HW:])
        put(6, dkf_ref[:, HW:].astype(F32) + dkb_ref[:, HW:].astype(F32))
        dp_ref[:, c0 + 4 * HW:c0 + 5 * HW] = dgo_ref[:, :HW]
        dp_ref[:, c0 + 8 * HW:c0 + 9 * HW] = dgo_ref[:, HW:]
        lr = lr_ref[...].astype(BF16)
        xg = _dot(lr, wgk_ref[...], NN) + bgk_ref[...]
        dxg = []
        for d, (raw_ref, dk_ref, dg_ref) in enumerate(((hf_ref, dkf_ref, dgf_ref), (hb_ref, dkb_ref, dgb_ref))):
            lbd = _hg_lb(lb_ref, d)
            s = _sig(raw_ref[...].astype(F32))
            f = lbd + (1.0 - lbd) * s
            df = dg_ref[:, :HW] / f - dk_ref[:, :HW].astype(F32)
            put(2 + d, df * (1.0 - lbd) * s * (1.0 - s))
            dlb_ref[d:d + 1, :] += _colsum(df * (1.0 - s)) * (lbd * (1.0 - lbd))
            dxg.append(dg_ref[:, HW:] * (1.0 / GLA_NORM) * _sig(-xg[:, d * HW:(d + 1) * HW]))
        dxg = jnp.concatenate(dxg, axis=1)
        db_ref[0:1, :] += _colsum(dxg)
        dxg_b = dxg.astype(BF16)
        dw_ref[...] += _dot(lr, dxg_b, TN)
        dlr = _dot(dxg_b, wgk_ref[...], NT)
        dp_ref[:, LR0:LR0 + DH] = (dlr + dgm_ref[:, :DH].astype(F32)).astype(BF16)
        dp_ref[:, LR0 + DH:GATE_GLA0] = dgm_ref[:, DH:D]
        dp_ref[:, GATE_GLA0:GATE_GLA0 + DH] = dgm_ref[:, D:GW] + dgm_ref[:, GW:GW + DH]
        dp_ref[:, GATE_GLA0 + DH:GATE_GLA0 + GW] = dgm_ref[:, GW + DH:]
        dp_ref[:, GATE_GLA0 + GW:] = jnp.zeros((TR, W_IN_COLS - GATE_GLA0 - GW), BF16)

    return pl.pallas_call(
        body, name="gates_bwd", grid=(t // TR,),
        in_specs=[seg(0), seg(2), seg(3), _rowcol(DH, LR0 // DH), _full((2, 2, HW)), _full((DH, D)), _full((1, D)),
                  _row(2 * GW), _row(D)] + [_row(D)] * 8,
        out_specs=[_row(W_IN_COLS), _full((8, HW)), _full((DH, D)), _full((8, D))],
        out_shape=[jax.ShapeDtypeStruct((t, W_IN_COLS), BF16), jax.ShapeDtypeStruct((8, HW), F32),
                   jax.ShapeDtypeStruct((DH, D), F32), jax.ShapeDtypeStruct((8, D), F32)],
        compiler_params=_cp("arbitrary"),
    )(p, p, p, p, hg_lb, wgk, bgk, dgm, dgo, dq_f, dq_b, dv_f, dv_b, dk_f, dk_b, dg_f, dg_b)


def _scan_consts(rev):
    r = lax.broadcasted_iota(jnp.int32, (CHUNK, CHUNK), 0)
    u = lax.broadcasted_iota(jnp.int32, (CHUNK, CHUNK), 1)
    rp = lax.broadcasted_iota(jnp.int32, (CHUNK, 1), 0)
    if rev:
        r, u, rp = CHUNK - 1 - r, CHUNK - 1 - u, CHUNK - 1 - rp
    tri = jnp.where(u <= r, 1.0, 0.0).astype(F32)
    tri_t = jnp.where(r <= u, 1.0, 0.0).astype(F32)
    lv = []
    for b in LEVELS:
        sh = b.bit_length() - 1
        pair = ((r >> sh) == (u >> sh) + 1) & (((u >> sh) & 1) == 0)
        pair_t = ((u >> sh) == (r >> sh) + 1) & (((r >> sh) & 1) == 0)
        tside = ((rp >> sh) & 1) == 1
        lv.append((pair, pair_t, tside, jnp.where(tside, 1.0, -1.0).astype(F32)))
    bd = LEVELS[-1].bit_length() - 1
    diag = ((r >> bd) == (u >> bd)) & (u <= r)
    diag_t = ((r >> bd) == (u >> bd)) & (r <= u)
    return tri, tri_t, lv, diag, diag_t


def _row_of(pos, rev):
    return CHUNK - 1 - pos if rev else pos


def _chunk_terms(cum, b_scr, consts, rev):
    _, _, lv, _, _ = consts
    terms = []
    for b, (_, _, _, sgn) in zip(LEVELS, lv):
        pieces = []
        for j in range(CHUNK // (2 * b)):
            row = _row_of(2 * b * j + b - 1, rev)
            pieces.append(jnp.broadcast_to(b_scr[row:row + 1, :], (2 * b, DH)))
        if rev:
            pieces = pieces[::-1]
        bnd = pieces[0] if len(pieces) == 1 else jnp.concatenate(pieces, axis=0)
        terms.append(jnp.exp((cum - bnd) * sgn))
    b = LEVELS[-1]
    pieces = []
    for j in range(CHUNK // b):
        if j == 0:
            pieces.append(jnp.zeros((b, DH), F32))
        else:
            row = _row_of(b * j - 1, rev)
            pieces.append(jnp.broadcast_to(b_scr[row:row + 1, :], (b, DH)))
    if rev:
        pieces = pieces[::-1]
    start = jnp.concatenate(pieces, axis=0)
    wq = jnp.exp(jnp.minimum(cum - start, 0.0))
    wk = jnp.exp(jnp.minimum(start - cum, EXP_CLAMP))
    terms.append((wq, wk))
    return terms


def _run_staged(units):
    live = list(units)
    while live:
        nxt = []
        for u in live:
            try:
                next(u)
                nxt.append(u)
            except StopIteration:
                pass
        live = nxt


SCAN_TB = 256
SCAN_CB = SCAN_TB // CHUNK


def _block_order(i, ntb, rev):
    nctx = CTX // SCAN_TB
    if not rev:
        return i
    return jnp.where(i < nctx, nctx - 1 - i, ntb - 1 - (i - nctx))


def _chunk_in_block(j, rev):
    return SCAN_CB - 1 - j if rev else j


def _scan_fwd(q, k, v, g, rev):
    t = q.shape[0]
    nc = t // CHUNK
    hpb = SCAN_HEADS_FWD

    def body(q_ref, k_ref, v_ref, g_ref, o_ref, st_ref, s_scr, b_scr):
        consts = _scan_consts(rev)
        _, _, lv, diag, _ = consts
        masks = [lvl[0] for lvl in lv] + [diag]

        @pl.when(pl.program_id(1) == 0)
        def _():
            s_scr[...] = jnp.zeros_like(s_scr)

        tri = consts[0]
        state = {hh: s_scr[hh] for hh in range(hpb)}

        def unit(hh, j):
            sl = slice(hh * DH, (hh + 1) * DH)
            c = _chunk_in_block(j, rev)
            rows = slice(c * CHUNK, (c + 1) * CHUNK)
            b_ref = b_scr.at[hh * SCAN_CB + j]
            qc, kc, vc, gc = q_ref[rows, sl], k_ref[rows, sl], v_ref[rows, sl], g_ref[rows, sl]
            cum = _split_dot(tri, gc)
            b_ref[...] = cum
            yield
            terms = _chunk_terms(cum, b_ref, consts, rev)
            qf, kf = qc.astype(F32), kc.astype(F32)
            xs = [(jnp.where(tside, qf, kf) * w).astype(BF16) for w, (_, _, tside, _) in zip(terms[:-1], lv)]
            qd, kd = (qf * terms[-1][0]).astype(BF16), (kf * terms[-1][1]).astype(BF16)
            tot = _colsum(gc)
            qe = (qf * jnp.exp(cum)).astype(BF16)
            ke = (kf * jnp.exp(tot - cum)).astype(BF16)
            vb = vc.astype(BF16)
            yield
            scs = [_dot(x, x, NT) for x in xs] + [_dot(qd, kd, NT)]
            kv = _dot(vb, ke, TN)
            yield
            a = jnp.zeros((CHUNK, CHUNK), F32)
            for sc, m in zip(scs, masks):
                a = a + jnp.where(m, sc, 0.0)
            o_intra = _dot(a.astype(BF16), vb, NN)
            yield
            st = state[hh]
            st_ref[hh, c] = st
            o_ref[rows, sl] = o_intra + _dot(qe, st.astype(BF16), NT)
            state[hh] = st * jnp.exp(tot) + kv
            yield

        _run_staged([unit(hh, j) for hh in range(hpb) for j in range(SCAN_CB)])
        for hh in range(hpb):
            s_scr[hh] = state[hh]

    ntb = t // SCAN_TB
    col = pl.BlockSpec((SCAN_TB, hpb * DH), lambda h, i: (_block_order(i, ntb, rev), h))
    return pl.pallas_call(
        body, name="scan_fwd_" + ("bw" if rev else "fw"), grid=(NH // hpb, ntb),
        in_specs=[col] * 4,
        out_specs=[col, pl.BlockSpec((hpb, SCAN_CB, DH, DH), lambda h, i: (h, _block_order(i, ntb, rev), 0, 0))],
        out_shape=[jax.ShapeDtypeStruct((t, D), F32), jax.ShapeDtypeStruct((NH, nc, DH, DH), F32)],
        scratch_shapes=[pltpu.VMEM((hpb, DH, DH), F32), pltpu.VMEM((hpb * SCAN_CB, CHUNK, DH), F32)],
        compiler_params=_cp("parallel", "arbitrary"),
    )(q, k, v, g)


def _scan_bwd(q, k, v, g, do, states, rev):
    t = q.shape[0]
    nc = t // CHUNK
    hpb = SCAN_HEADS_BWD

    def body(q_ref, k_ref, v_ref, g_ref, do_ref, st_ref, dq_ref, dk_ref, dv_ref, dg_ref, ds_scr, b_scr):
        consts = _scan_consts(rev)
        _, tri_t, lv, diag, diag_t = consts
        masks = [(lvl[0], lvl[1]) for lvl in lv] + [(diag, diag_t)]
        @pl.when(pl.program_id(1) == 0)
        def _():
            ds_scr[...] = jnp.zeros_like(ds_scr)

        tri = consts[0]
        dstate = {hh: ds_scr[hh] for hh in range(hpb)}

        def unit(hh, jj):
            sl = slice(hh * DH, (hh + 1) * DH)
            c = _chunk_in_block(SCAN_CB - 1 - jj, rev)
            rows = slice(c * CHUNK, (c + 1) * CHUNK)
            b_ref = b_scr.at[hh * SCAN_CB + jj]
            qc, kc, vc, gc = q_ref[rows, sl], k_ref[rows, sl], v_ref[rows, sl], g_ref[rows, sl]
            dob = do_ref[rows, sl].astype(BF16)
            vb = vc.astype(BF16)
            cum = _split_dot(tri, gc)
            b_ref[...] = cum
            da = _dot(dob, vb, NT)
            da_t = _dot(vb, dob, NT)
            yield
            terms = _chunk_terms(cum, b_ref, consts, rev)
            qf, kf = qc.astype(F32), kc.astype(F32)
            xs = [(jnp.where(tside, qf, kf) * w).astype(BF16) for w, (_, _, tside, _) in zip(terms[:-1], lv)]
            wqd, wkd = terms[-1]
            qdb, kdb = (qf * wqd).astype(BF16), (kf * wkd).astype(BF16)
            tot = _colsum(gc)
            e_tot = jnp.exp(tot)
            e_b = jnp.exp(cum)
            e_t = jnp.exp(tot - cum)
            qeb = (qf * e_b).astype(BF16)
            keb = (kf * e_t).astype(BF16)
            dsym = [(jnp.where(m, da, 0.0) + jnp.where(m_t, da_t, 0.0)).astype(BF16) for m, m_t in masks[:-1]]
            dad = (jnp.where(diag, da, 0.0).astype(BF16), jnp.where(diag_t, da_t, 0.0).astype(BF16))
            yield
            sym = [_dot(x, x, NT) for x in xs]
            dxs = [_dot(d, x, NN) for d, x in zip(dsym, xs)]
            at_d = _dot(kdb, qdb, NT)
            dqt_d = _dot(dad[0], kdb, NN)
            dkt_d = _dot(dad[1], qdb, NN)
            qd = _dot(dob, qeb, TN)
            yield
            a_t = jnp.where(diag_t, at_d, 0.0)
            dq = dqt_d * wqd
            dk = dkt_d * wkd
            db = dqt_d * qdb.astype(F32) - dkt_d * kdb.astype(F32)
            for s, dx, x, w, (_, m_t, tside, sgn) in zip(sym, dxs, xs, terms[:-1], lv):
                a_t = a_t + jnp.where(m_t, s, 0.0)
                dxw = dx * w
                dq = dq + jnp.where(tside, dxw, 0.0)
                dk = dk + jnp.where(tside, 0.0, dxw)
                db = db + (dx * x.astype(F32)) * sgn
            dv_intra = _dot(a_t.astype(BF16), dob, NN)
            st = st_ref[hh, c]
            stb = st.astype(BF16)
            dqe = _dot(dob, stb, NN)
            yield
            dst = dstate[hh]
            dstb = dst.astype(BF16)
            dstate[hh] = dst * e_tot + qd
            dv_ref[rows, sl] = (dv_intra + _dot(keb, dstb, NT)).astype(BF16)
            dke = _dot(vb, dstb, NN)
            yield
            qe = qeb.astype(F32)
            ke = keb.astype(F32)
            dq_ref[rows, sl] = (dq + dqe * e_b).astype(BF16)
            dk_ref[rows, sl] = (dk + dke * e_t).astype(BF16)
            db = db + dqe * qe - dke * ke
            dtot = _colsum(dstb.astype(F32) * stb.astype(F32)) * e_tot + _colsum(dke * ke)
            dg_ref[rows, sl] = _split_dot(tri_t, db) + dtot
            yield

        _run_staged([unit(hh, jj) for hh in range(hpb) for jj in range(SCAN_CB)])
        for hh in range(hpb):
            ds_scr[hh] = dstate[hh]

    ntb = t // SCAN_TB
    blk = lambda i: _block_order(ntb - 1 - i, ntb, rev)
    col = pl.BlockSpec((SCAN_TB, hpb * DH), lambda h, i: (blk(i), h))
    out = jax.ShapeDtypeStruct((t, D), F32)
    outb = jax.ShapeDtypeStruct((t, D), BF16)
    return pl.pallas_call(
        body, name="scan_bwd_" + ("bw" if rev else "fw"), grid=(NH // hpb, ntb),
        in_specs=[col] * 5 + [pl.BlockSpec((hpb, SCAN_CB, DH, DH), lambda h, i: (h, blk(i), 0, 0))],
        out_specs=[col] * 4,
        out_shape=[outb] * 3 + [out],
        scratch_shapes=[pltpu.VMEM((hpb, DH, DH), F32), pltpu.VMEM((hpb * SCAN_CB, CHUNK, DH), F32)],
        compiler_params=_cp("parallel", "arbitrary"),
    )(q, k, v, g, do, states)


W_IN_GRAD_CHUNKS = (("a", (0, 512)), ("b", (0, 256)), ("b", (256, 512)))
W_IN_REF = 6688


def _layout_w_in(w):
    return jnp.pad(w, ((0, 0), (0, W_IN_COLS - W_IN_REF)))


def _unlayout_w_in(d):
    return d[:, :W_IN_REF]


W_IN_PAD = 896


def _assemble_w_in(g):
    n, r, wp = g.shape
    tr = 256
    tiles = wp // DH

    def body(g_ref, o_ref):
        lane = lax.broadcasted_iota(jnp.int32, (tr, DH), 1)
        for t in range(W_IN_COLS // DH):
            acc = None
            for j in range(n):
                c = DH * t - W_IN_SHARD * j
                if c <= -DH or c >= W_IN_SHARD:
                    continue
                k, s = divmod(c, DH)
                lo = g_ref[j, :, k * DH:(k + 1) * DH] if 0 <= k < tiles else None
                hi = g_ref[j, :, (k + 1) * DH:(k + 2) * DH] if s and 0 <= k + 1 < tiles else None
                if s:
                    zero = jnp.zeros((tr, DH), g.dtype)
                    lo = zero if lo is None else pltpu.roll(lo, DH - s, 1)
                    hi = zero if hi is None else pltpu.roll(hi, DH - s, 1)
                    part = jnp.where(lane < DH - s, lo, hi)
                else:
                    part = lo
                acc = part if acc is None else acc + part
            o_ref[:, t * DH:(t + 1) * DH] = jnp.zeros((tr, DH), g.dtype) if acc is None else acc

    return pl.pallas_call(
        body, name="assemble_w_in", grid=(r // tr,),
        in_specs=[pl.BlockSpec((n, tr, wp), lambda i: (0, i, 0))],
        out_specs=pl.BlockSpec((tr, W_IN_COLS), lambda i: (i, 0)),
        out_shape=jax.ShapeDtypeStruct((r, W_IN_COLS), g.dtype),
        compiler_params=_cp("parallel"),
    )(g)


def _gate_cols(w):
    return jnp.pad(w, ((0, 0), (GOFF, GW - GOFF - D)))


def _gate_rows(w):
    return jnp.pad(w, ((GOFF, GW - GOFF - D), (0, 0)))


def _layout_wgk(w):
    r = w.shape[1]
    top = jnp.concatenate([w[0], jnp.zeros_like(w[0])], axis=1)
    bot = jnp.concatenate([jnp.zeros_like(w[1]), w[1]], axis=1)
    return jnp.concatenate([top, bot, jnp.zeros((DH - 2 * r, D), w.dtype)], axis=0)


def _unlayout_wgk(d, r=16):
    return jnp.stack([d[:r, :HW], d[r:2 * r, HW:]])


def _local_step(z, target, modc, modx, norms, onw, hg_lb, wgk, bgk, get_w_in, get_mix, get_ffn, send):
    n_pre1, n_post1, n_pre2, n_post2 = norms
    t = z[0].shape[0] + z[1].shape[0]
    tm = 1152 if t % 1152 == 0 else 256
    h1 = _prenorm(z, n_pre1, modc, modx, 0, 1, "prenorm1")
    w_in = get_w_in(h1)
    p = _matmul(h1, w_in, NN, BF16, "mm_in", t, 1024, D)
    q, v, k_f, k_b, g_f, g_b = _gates_fwd(p, hg_lb, wgk, bgk)
    o_f, st_f = _scan_fwd(q, k_f, v, g_f, False)
    o_b, st_b = _scan_fwd(q, k_b, v, g_b, True)
    y = _post_fwd(o_f, o_b, p, onw)
    w_br_hg, w_br_gla, w_out = get_mix(y)
    u1 = _matmul(y, w_br_hg, NN, BF16, "mm_br_hg", tm, GW, HW, a_off=0)
    u2 = _matmul(y, w_br_gla, NN, BF16, "mm_br_gla", tm, GW, HW, a_off=1)
    merged = _merge_fwd(p, u1, u2)
    y1 = _matmul(merged, w_out, NN, BF16, "mm_out", tm, 512, GW)
    z1, h2 = _mid_fwd(z, y1, n_post1, n_pre2, modc, modx)
    w_gu_t, w_down = get_ffn(h2)
    uv = _matmul(h2, w_gu_t, NT, BF16, "mm_gu", t, D_FF // 2, D)
    act = _swiglu_fwd(uv)
    y2 = _matmul(act, w_down, NN, BF16, "mm_down", t, 512, D_FF)
    dz, dy2, loss_vec, sm_final = _final(z1, y2, target, n_post2, modc, modx)
    dact = _matmul(dy2, w_down, NT, BF16, "mm_down_dx", t, D_FF // 2, D)
    d_w_down = _matmul(_transpose(dy2, "t_dy2"), act, NN, BF16, "mm_down_dw", 512, D_FF // 2, t, out_t=True)
    duv = _swiglu_bwd(uv, dact)
    dh2 = _matmul(duv, w_gu_t, NN, BF16, "mm_gu_dx", tm, 512, D_FF)
    h2_t = _transpose(h2, "t_h2")
    d_w_gate_t = _matmul(h2_t, duv, NN, BF16, "mm_gate_dw", 512, D_FF // 2, t, b_coff=0, n_out=D_FF, out_t=True)
    d_w_up_t = _matmul(h2_t, duv, NN, BF16, "mm_up_dw", 512, D_FF // 2, t, b_coff=2, n_out=D_FF, out_t=True)
    dh2 = send(("w_down", "w_gate_t", "w_up_t"), (d_w_down, d_w_gate_t, d_w_up_t), dh2)
    dz, dy1, sm_mid = _mid_bwd(dh2, dz, z1, y1, n_post1, n_pre2, modc, modx)
    dmerged = _matmul(dy1, w_out, NT, BF16, "mm_out_dx", tm, GW, D)
    d_w_out = _matmul(_transpose(dy1, "t_dy1"), merged, NN, BF16, "mm_out_dw", 512, GW, t, out_t=True)
    du1, du2, dgm = _merge_bwd(dmerged, p, u1, u2)
    dy_hg = _matmul(du1, w_br_hg, NT, BF16, "mm_br_hg_dx", tm, HW, GW)
    dy_gla = _matmul(du2, w_br_gla, NT, BF16, "mm_br_gla_dx", tm, HW, GW)
    y_t = _transpose(y, "t_y")
    d_w_br_hg = _matmul(y_t, du1, NN, BF16, "mm_br_hg_dw", HW, GW, t, a_roff=0, m_out=HW)
    d_w_br_gla = _matmul(y_t, du2, NN, BF16, "mm_br_gla_dw", HW, GW, t, a_roff=1, m_out=HW)
    dy_hg = send(("w_out", "w_br_hg", "w_br_gla"), (d_w_out, d_w_br_hg, d_w_br_gla), dy_hg)
    do, dgo, sm_post = _post_bwd(dy_hg, dy_gla, o_f, o_b, p, onw)
    dq_f, dk_f, dv_f, dg_f = _scan_bwd(q, k_f, v, g_f, do, st_f, False)
    dq_b, dk_b, dv_b, dg_b = _scan_bwd(q, k_b, v, g_b, do, st_b, True)
    dp, d_lb, d_wgk, d_bgk = _gates_bwd(p, hg_lb, wgk, bgk, dgm, dgo, dq_f, dq_b, dv_f, dv_b, dk_f, dk_b, dg_f, dg_b)
    h1_t = _transpose(h1, "t_h1")
    d_w_in_a = _matmul(h1_t, dp, NN, BF16, "mm_in_dw_a", 512, 1024, t, a_roff=0, m_out=D // 2)
    dp = send(("w_in_a",), (d_w_in_a,), dp)
    d_w_in_b = _matmul(h1_t, dp, NN, BF16, "mm_in_dw_b", 512, 1024, t, a_roff=1, m_out=D // 2)
    dp = send(("w_in_b",), (d_w_in_b,), dp)
    dh1 = _matmul(dp, w_in, NT, BF16, "mm_in_dx", tm, 512, W_IN_COLS // 2)
    grad_x, sm_pre = _pre_bwd(dh1, dz, z, n_pre1, modc, modx)
    return dict(loss_vec=loss_vec, grad_x=grad_x, sm_final=sm_final, sm_mid=sm_mid, sm_post=sm_post, sm_pre=sm_pre,
                d_lb=d_lb, d_wgk=d_wgk, d_bgk=d_bgk)


MESH = pl.DeviceIdType.MESH
ANY = pl.BlockSpec(memory_space=pl.ANY)
N_REL = N_DEV - 1


def _place():
    return lax.axis_index("x"), lax.axis_index("y"), lax.axis_index("c")


def _slot(p):
    return 4 * p[0] + 2 * p[1] + p[2]


def _all_gather(arrays, name):
    n = len(arrays)

    def body(*refs):
        ins, outs = refs[:n], refs[n:2 * n]
        send_sems, recv_sems, local_sems = refs[2 * n:]
        x, y, c = _place()
        me, sibling = (x, y, c), (x, y, 1 - c)
        chips = [(1 - x, y), (x, 1 - y), (1 - x, 1 - y)]

        def copy(a, k, block, to, src=None):
            dst = outs[a].at[_slot(block)]
            return pltpu.make_async_remote_copy(
                src_ref=dst if src is None else src, dst_ref=dst,
                send_sem=send_sems.at[N_REL * a + k], recv_sem=recv_sems.at[N_REL * a + k],
                device_id=to, device_id_type=MESH)

        mine = [pltpu.make_async_copy(ins[a], outs[a].at[_slot(me)], local_sems.at[a]) for a in range(n)]
        for cp in mine:
            cp.start()
        first = []
        for a in range(n):
            first.append(copy(a, 0, me, sibling, src=ins[a]))
            first += [copy(a, 1 + j, me, (*chip, c), src=ins[a]) for j, chip in enumerate(chips)]
        for cp in first:
            cp.start()
        passed = []
        for j, chip in enumerate(chips):
            for a in range(n):
                copy(a, 1 + j, (*chip, c), me).wait_recv()
                fwd = copy(a, 4 + j, (*chip, c), sibling)
                fwd.start()
                passed.append(fwd)
        for a in range(n):
            copy(a, 0, sibling, me).wait_recv()
        for j, chip in enumerate(chips):
            for a in range(n):
                copy(a, 4 + j, (*chip, 1 - c), me).wait_recv()
        for cp in first + passed:
            cp.wait_send()
        for cp in mine:
            cp.wait()

    return pl.pallas_call(
        body, name=name,
        in_specs=[ANY] * n, out_specs=[ANY] * n,
        out_shape=[jax.ShapeDtypeStruct((N_DEV,) + a.shape, a.dtype) for a in arrays],
        scratch_shapes=[pltpu.SemaphoreType.DMA((N_REL * n,)), pltpu.SemaphoreType.DMA((N_REL * n,)),
                        pltpu.SemaphoreType.DMA((n,))],
    )(*arrays)


def _exchange(arrays, name):
    n = len(arrays)

    def body(*refs):
        ins, outs = refs[:n], refs[n:2 * n]
        send_sems, recv_sems, local_sems = refs[2 * n:]
        x, y, c = _place()
        me = _slot((x, y, c))
        mine = [pltpu.make_async_copy(ins[a].at[me], outs[a].at[me], local_sems.at[a]) for a in range(n)]
        for cp in mine:
            cp.start()
        copies = []
        for a in range(n):
            for k in range(1, N_DEV):
                flip = lambda v, bit: 1 - v if bit else v
                peer = (flip(x, k & 4), flip(y, k & 2), flip(c, k & 1))
                copies.append(pltpu.make_async_remote_copy(
                    src_ref=ins[a].at[_slot(peer)], dst_ref=outs[a].at[me],
                    send_sem=send_sems.at[N_REL * a + k - 1], recv_sem=recv_sems.at[N_REL * a + k - 1],
                    device_id=peer, device_id_type=MESH))
                copies[-1].start()
        i = 0
        for a in range(n):
            for k in range(1, N_DEV):
                flip = lambda v, bit: 1 - v if bit else v
                peer = (flip(x, k & 4), flip(y, k & 2), flip(c, k & 1))
                pltpu.make_async_remote_copy(
                    src_ref=ins[a].at[_slot(peer)], dst_ref=outs[a].at[_slot(peer)],
                    send_sem=send_sems.at[N_REL * a + k - 1], recv_sem=recv_sems.at[N_REL * a + k - 1],
                    device_id=peer, device_id_type=MESH).wait_recv()
                i += 1
        for cp in copies:
            cp.wait_send()
        for cp in mine:
            cp.wait()

    return pl.pallas_call(
        body, name=name,
        in_specs=[ANY] * n, out_specs=[ANY] * n,
        out_shape=[jax.ShapeDtypeStruct(a.shape, a.dtype) for a in arrays],
        scratch_shapes=[pltpu.SemaphoreType.DMA((N_REL * n,)), pltpu.SemaphoreType.DMA((N_REL * n,)),
                        pltpu.SemaphoreType.DMA((n,))],
    )(*arrays)


HBM = pl.BlockSpec(memory_space=pltpu.HBM)
SEM = pl.BlockSpec(memory_space=pltpu.SEMAPHORE)
EFFECT = pltpu.SideEffectType.DATAFLOW_SIDE_EFFECTING


def _peer_of(x, y, c, k):
    flip = lambda v, bit: 1 - v if bit else v
    return flip(x, k & 4), flip(y, k & 2), flip(c, k & 1)


def _view_whole(src, slot):
    return src


def _view_near(src, slot):
    return src


_view_near.peers = (1, 2, 4, 6)


def _view_block(src, slot):
    return src.at[slot]


W_IN_SHARD = W_IN_REF // N_DEV


def _view_window(rows):
    def view(src, slot):
        col0 = pl.multiple_of((W_IN_SHARD * slot // DH) * DH, DH)
        return src.at[pl.ds(rows[0], rows[1] - rows[0]), pl.ds(col0, D)]
    return view


def _split_copies(view, srcs, lands, send_sems, recv_sems, local_sems):
    x, y, c = _place()
    me = _slot((x, y, c))
    local, sends, waits = [], [], []
    for a, (src, land) in enumerate(zip(srcs, lands)):
        local.append(pltpu.make_async_copy(view(src, me), land.at[me], local_sems.at[a]))
        for k in getattr(view, "peers", range(1, N_DEV)):
            peer = _peer_of(x, y, c, k)
            mine = view(src, _slot(peer))
            sems = dict(send_sem=send_sems.at[N_REL * a + k - 1], recv_sem=recv_sems.at[N_REL * a + k - 1],
                        device_id=peer, device_id_type=MESH)
            sends.append(pltpu.make_async_remote_copy(src_ref=mine, dst_ref=land.at[me], **sems))
            waits.append(pltpu.make_async_remote_copy(src_ref=mine, dst_ref=land.at[_slot(peer)], **sems))
    return local, sends, waits


def _split_start(view, land_shapes, srcs, name, after):
    n = len(srcs)
    lands = [lax.empty(shp, s.dtype) for shp, s in zip(land_shapes, srcs)]

    def body(*refs):
        src_refs, land_refs = refs[:n], refs[n:2 * n]
        send_sems, recv_sems, local_sems = refs[2 * n + 1:2 * n + 4]
        token = refs[-1]
        local, sends, _ = _split_copies(view, src_refs, land_refs, send_sems, recv_sems, local_sems)
        for cp in local + sends:
            cp.start()
        token[...] = jnp.zeros_like(token)

    hbm = lambda a: pltpu.with_memory_space_constraint(a, pltpu.HBM)
    out = pl.pallas_call(
        body, name=name,
        out_shape=(pltpu.SemaphoreType.DMA((N_REL * n,)), pltpu.SemaphoreType.DMA((N_REL * n,)),
                   pltpu.SemaphoreType.DMA((n,)),
                   *[pltpu.HBM(s.shape, s.dtype) for s in srcs], *[pltpu.HBM(l.shape, l.dtype) for l in lands],
                   jax.ShapeDtypeStruct((8, DH), F32)),
        in_specs=[HBM] * (2 * n) + [ANY],
        out_specs=(SEM, SEM, SEM, *([HBM] * (2 * n)), pl.BlockSpec(memory_space=pltpu.VMEM)),
        input_output_aliases={i: 3 + i for i in range(2 * n)},
        compiler_params=pltpu.CompilerParams(has_side_effects=EFFECT),
    )(*[hbm(s) for s in srcs], *[hbm(l) for l in lands], after)
    handle = dict(view=view, n=n, sems=out[:3], srcs=list(out[3:3 + n]), lands=list(out[3 + n:3 + 2 * n]))
    return handle, out[-1]


def _split_wait(handle, name, after, srcs=None):
    view, n, sems, lands = handle["view"], handle["n"], handle["sems"], handle["lands"]
    srcs = handle["srcs"] if srcs is None else srcs
    afters = list(after) if isinstance(after, (list, tuple)) else [after]

    def body(*refs):
        src_refs, land_refs = refs[:n], refs[n:2 * n]
        send_sems, recv_sems, local_sems = refs[2 * n:2 * n + 3]
        local, _, waits = _split_copies(view, src_refs, land_refs, send_sems, recv_sems, local_sems)
        for cp in waits:
            cp.wait_send()
            cp.wait_recv()
        for cp in local:
            cp.wait()

    out = pl.pallas_call(
        body, name=name,
        out_shape=(*[pltpu.HBM(s.shape, s.dtype) for s in srcs], *[pltpu.HBM(l.shape, l.dtype) for l in lands]),
        in_specs=[HBM] * (2 * n) + [SEM, SEM, SEM] + [ANY] * len(afters),
        out_specs=tuple([HBM] * (2 * n)),
        input_output_aliases={i: i for i in range(2 * n)},
        compiler_params=pltpu.CompilerParams(has_side_effects=EFFECT),
    )(*srcs, *lands, *sems, *afters)
    handle["srcs"] = list(out[:n])
    return list(out[n:])


def _tie(x, token, name):
    def body(x_ref, t_ref, o_ref):
        pass

    return pl.pallas_call(
        body, name=name, out_shape=jax.ShapeDtypeStruct(x.shape, x.dtype),
        in_specs=[ANY, ANY], out_specs=ANY, input_output_aliases={0: 0},
    )(x, token)


def _forward_to_sibling(land, name):
    def body(land_ref, out_ref, send_sems, recv_sems):
        x, y, c = _place()
        sibling = (x, y, 1 - c)
        chips = [(1 - x, y), (x, 1 - y), (1 - x, 1 - y)]

        def copy(j, core):
            blk = _slot((*chips[j], core))
            return pltpu.make_async_remote_copy(src_ref=land_ref.at[blk], dst_ref=out_ref.at[blk],
                                                send_sem=send_sems.at[j], recv_sem=recv_sems.at[j],
                                                device_id=sibling, device_id_type=MESH)

        sends = [copy(j, c) for j in range(3)]
        for cp in sends:
            cp.start()
        for j in range(3):
            copy(j, 1 - c).wait_recv()
        for cp in sends:
            cp.wait_send()

    return pl.pallas_call(
        body, name=name, in_specs=[ANY], out_specs=ANY, input_output_aliases={0: 0},
        out_shape=jax.ShapeDtypeStruct(land.shape, land.dtype),
        scratch_shapes=[pltpu.SemaphoreType.DMA((3,)), pltpu.SemaphoreType.DMA((3,))],
    )(land)


def _mod_fwd(a, w, b):
    def body(a_ref, w_ref, b_ref, o_ref):
        o_ref[...] = _dot(_silu(a_ref[...]), w_ref[...], NN, precision=HI) + b_ref[...]

    return pl.pallas_call(
        body, name="mod_fwd", out_shape=jax.ShapeDtypeStruct((a.shape[0], w.shape[1]), F32),
        compiler_params=pltpu.CompilerParams(vmem_limit_bytes=VMEM_LIMIT),
    )(a, w, b)


def _mod_bwd(a, d, w):
    def body(a_ref, d_ref, w_ref, dw_ref, dc_ref):
        av = a_ref[...]
        dv = d_ref[...]
        dw_ref[...] = _dot(_silu(av), dv, TN, precision=HI)
        da = _dot(dv[0:8, :], w_ref[...], NT, precision=HI) * _dsilu(av[0:8, :])
        row = lax.broadcasted_iota(jnp.int32, da.shape, 0)
        dc_ref[...] = jnp.where(row == 0, da, 0.0)

    return pl.pallas_call(
        body, name="mod_bwd",
        out_shape=[jax.ShapeDtypeStruct(w.shape, F32), jax.ShapeDtypeStruct((8, w.shape[0]), F32)],
        compiler_params=pltpu.CompilerParams(vmem_limit_bytes=VMEM_LIMIT),
    )(a, d, w)


def _sum_devices(g):
    def body(g_ref, o_ref):
        acc = g_ref[0]
        for i in range(1, g.shape[0]):
            acc = acc + g_ref[i]
        o_ref[...] = acc

    return pl.pallas_call(body, name="sum_devices_%d" % g.shape[1],
                          out_shape=jax.ShapeDtypeStruct(g.shape[1:], F32))(g)


def _sum_windows(g, name):
    n, r, c = g.shape
    tr = 128

    def body(g_ref, o_ref):
        x, y, cc = _place()
        lane0 = (W_IN_SHARD * _slot((x, y, cc))) % DH
        acc = g_ref[0].astype(F32)
        for i in range(1, n):
            acc = acc + g_ref[i].astype(F32)
        o_ref[...] = pltpu.roll(acc, (c - lane0) % c, 1).T

    return pl.pallas_call(
        body, name=name, grid=(r // tr,),
        in_specs=[pl.BlockSpec((n, tr, c), lambda i: (0, i, 0))],
        out_specs=pl.BlockSpec((c, tr), lambda i: (0, i)),
        out_shape=jax.ShapeDtypeStruct((c, r), F32),
        compiler_params=_cp("parallel"),
    )(g)


def _adam_rows(r, c, n):
    budget = 6 * 1024 * 1024
    best = None
    for tr in range(16, r + 1, 16):
        if r % tr == 0 and tr * c * (2 * n + 28) <= budget:
            best = tr
    return best if best is not None else r


def _adamw(g, w, m, v, name):
    n, r, c = g.shape
    tr = _adam_rows(r, c, n)
    bc1 = 1.0 - ADAM_B1 ** ADAM_STEP
    bc2 = 1.0 - ADAM_B2 ** ADAM_STEP

    def body(g_ref, w_ref, m_ref, v_ref, go_ref, d_ref, mo_ref, vo_ref):
        grad = g_ref[0].astype(F32)
        for i in range(1, n):
            grad = grad + g_ref[i].astype(F32)
        go_ref[...] = grad
        m_new = ADAM_B1 * m_ref[...] + (1.0 - ADAM_B1) * grad
        v_new = ADAM_B2 * v_ref[...] + (1.0 - ADAM_B2) * (grad * grad)
        mo_ref[...] = m_new
        vo_ref[...] = v_new
        d_ref[...] = -ADAM_LR * ((m_new / bc1) / (jnp.sqrt(v_new / bc2) + ADAM_EPS) + ADAM_WD * w_ref[...])

    blk = pl.BlockSpec((tr, c), lambda i: (i, 0))
    out = jax.ShapeDtypeStruct((r, c), F32)
    return pl.pallas_call(
        body, name=name, grid=(r // tr,),
        in_specs=[pl.BlockSpec((n, tr, c), lambda i: (0, i, 0)), blk, blk, blk],
        out_specs=[blk] * 4, out_shape=[out] * 4,
        compiler_params=_cp("parallel"),
    )(g, w, m, v)


ADAM_ROWS3 = 168


def _adam_math(grad, w, m, v):
    bc1 = 1.0 - ADAM_B1 ** ADAM_STEP
    bc2 = 1.0 - ADAM_B2 ** ADAM_STEP
    m_new = ADAM_B1 * m + (1.0 - ADAM_B1) * grad
    v_new = ADAM_B2 * v + (1.0 - ADAM_B2) * (grad * grad)
    delta = -ADAM_LR * ((m_new / bc1) / (jnp.sqrt(v_new / bc2) + ADAM_EPS) + ADAM_WD * w)
    return delta, m_new, v_new


def _adamw_rows3(g, w3, m3, v3, name):
    r, _, c = w3.shape
    n = ADAM_ROWS3
    starts = list(range(0, r - n, n)) + [r - n]

    def body(g_hbm, w_hbm, m_hbm, v_hbm, go_hbm, d_hbm, mo_hbm, vo_hbm, gbuf, ibuf, obuf, in_sems, out_sems):
        def fetch(p):
            r0, slot = starts[p], p % 2
            g0 = (r0 // 8) * 8
            cps = [pltpu.make_async_copy(g_hbm.at[pl.ds(g0, n + 8)], gbuf.at[slot], in_sems.at[slot, 0])]
            cps += [pltpu.make_async_copy(h.at[pl.ds(r0, n), 0], ibuf.at[slot, k], in_sems.at[slot, 1 + k])
                    for k, h in enumerate((w_hbm, m_hbm, v_hbm))]
            for cp in cps:
                cp.start()
            return cps

        pending, outs = fetch(0), []
        for p, r0 in enumerate(starts):
            slot = p % 2
            nxt = fetch(p + 1) if p + 1 < len(starts) else []
            for cp in pending:
                cp.wait()
            grad = gbuf[slot, pl.ds(r0 - (r0 // 8) * 8, n), :]
            delta, m_new, v_new = _adam_math(grad, ibuf[slot, 0], ibuf[slot, 1], ibuf[slot, 2])
            for cp in outs:
                cp.wait()
            for k, val in enumerate((grad, delta, m_new, v_new)):
                obuf[slot, k] = val
            outs = [pltpu.make_async_copy(obuf.at[slot, k], h.at[pl.ds(r0, n), 0], out_sems.at[slot, k])
                    for k, h in enumerate((go_hbm, d_hbm, mo_hbm, vo_hbm))]
            for cp in outs:
                cp.start()
            pending = nxt
        for cp in outs:
            cp.wait()

    out = jax.ShapeDtypeStruct(w3.shape, F32)
    return pl.pallas_call(
        body, name=name, in_specs=[ANY] * 4, out_specs=[ANY] * 4, out_shape=[out] * 4,
        scratch_shapes=[pltpu.VMEM((2, n + 8, c), F32), pltpu.VMEM((2, 3, n, c), F32), pltpu.VMEM((2, 4, n, c), F32),
                        pltpu.SemaphoreType.DMA((2, 4)), pltpu.SemaphoreType.DMA((2, 4))],
        compiler_params=pltpu.CompilerParams(vmem_limit_bytes=VMEM_LIMIT),
    )(g, w3, m3, v3)


def kernel(x, c, ctx, c_ctx, w_mod, b_mod, norm_pre1, norm_post1, norm_pre2, norm_post2, w_in, hg_lb, hg_onorm, gla_w_gk, gla_b_gk, gla_onorm, w_br_hg, w_br_gla, w_out, w_ff_gate, w_ff_up, w_ff_down, loss_target, m_c_ctx, m_w_mod, m_b_mod, m_norm_pre1, m_norm_post1, m_norm_pre2, m_norm_post2, m_w_in, m_hg_lb, m_hg_onorm, m_gla_w_gk, m_gla_b_gk, m_gla_onorm, m_w_br_hg, m_w_br_gla, m_w_out, m_w_ff_gate, m_w_ff_up, m_w_ff_down, v_c_ctx, v_w_mod, v_b_mod, v_norm_pre1, v_norm_post1, v_norm_pre2, v_norm_post2, v_w_in, v_hg_lb, v_hg_onorm, v_gla_w_gk, v_gla_b_gk, v_gla_onorm, v_w_br_hg, v_w_br_gla, v_w_out, v_w_ff_gate, v_w_ff_up, v_w_ff_down):
    xi, yi, ci = lax.axis_index("x"), lax.axis_index("y"), lax.axis_index("c")
    me = 4 * xi + 2 * yi + ci
    t = CTX + x.shape[1]

    c_all, lb_g, wgk_g, bgk_g = _all_gather([c, hg_lb, gla_w_gk[0], gla_b_gk[0]], "ag_small")
    tr_ = lambda a: jnp.swapaxes(a[0], 0, 1)
    big = [w_in[0], w_br_hg[0], w_br_gla[0], w_out[0], tr_(w_ff_gate), tr_(w_ff_up), w_ff_down[0]]
    big_bf = [w.astype(BF16) for w in big]
    big_bf[0] = jnp.pad(big_bf[0], ((0, 0), (0, W_IN_PAD - W_IN_SHARD)))
    cols = lambda g: jnp.transpose(g, (1, 0, 2)).reshape(g.shape[1], N_DEV * g.shape[2])

    def get_w_in(after):
        land, = _split_wait(w_in_handle, "ag_w_in_wait", after)
        return _assemble_w_in(_forward_to_sibling(land, "ag_w_in_forward"))

    def get_mix(after):
        g_brh, g_brg, g_out = _split_wait(mix_handle, "ag_mix_wait", after)
        return _gate_cols(cols(g_brh)), _gate_cols(cols(g_brg)), _gate_rows(g_out.reshape(D, D))

    def get_ffn(after):
        g_gate, g_up, g_down = _split_wait(ffn_handle, "ag_ffn_wait", after)
        return (g_gate.reshape(D_FF, D), g_up.reshape(D_FF, D)), g_down.reshape(D_FF, D)

    hg_lb_full = jnp.transpose(lb_g, (1, 2, 0, 3)).reshape(2, 2, HW)
    wgk_k = _layout_wgk(jnp.transpose(wgk_g, (1, 2, 0, 3)).reshape(2, 16, HW)).astype(BF16)
    bgk_k = jnp.transpose(bgk_g, (1, 0, 2)).reshape(1, D)
    onw = jnp.concatenate([jnp.tile(hg_onorm, (1, NH // 2)), jnp.tile(gla_onorm, (1, NH // 2))], axis=1)

    n_mod = w_mod.shape[2]
    a9 = jnp.concatenate([c_ctx[None], c_all[:, 0], jnp.zeros((16 - 1 - N_DEV, D), F32)], axis=0)
    b_loc = lax.dynamic_slice(b_mod, (0, me * n_mod), (1, n_mod))
    s_loc = _mod_fwd(a9, w_mod[0], b_loc)
    s_all, = _all_gather([s_loc], "ag_mod")
    mod_all = jnp.transpose(s_all, (1, 0, 2)).reshape(16, N_DEV * n_mod)
    pad8 = lambda m: jnp.concatenate([m.reshape(6, D), jnp.zeros((2, D), F32)], axis=0)
    modc = pad8(mod_all[0])
    modx = pad8(lax.dynamic_slice(mod_all, (1 + me, 0), (1, N_DEV * n_mod))[0])

    gathered = lambda arrs: [(N_DEV,) + a.shape for a in arrs]
    w_in_handle, tok = _split_start(_view_near, gathered(big_bf[:1]), big_bf[:1], "ag_w_in_start", s_all)
    mix_handle, tok = _split_start(_view_whole, gathered(big_bf[1:4]), big_bf[1:4], "ag_mix_start", tok)
    ffn_handle, tok = _split_start(_view_whole, gathered(big_bf[4:]), big_bf[4:], "ag_ffn_start", tok)

    z = (ctx[0], x[0])
    modx = _tie(modx, tok, "tie_mod")
    norms = (norm_pre1, norm_post1, norm_pre2, norm_post2)
    shard = lambda d: jnp.transpose(d.reshape(d.shape[0], N_DEV, -1), (1, 0, 2)).astype(BF16)
    rowshard = lambda d: d.reshape(N_DEV, d.shape[0] // N_DEV, d.shape[1]).astype(BF16)
    sent, w_in_grad = [], {}

    def send_w_in(i, x_after):
        half, rows = W_IN_GRAD_CHUNKS[i]
        handle, tok = _split_start(_view_window(rows), [(N_DEV, rows[1] - rows[0], D)], w_in_grad[half],
                                   "grads_w_in%d_start" % i, x_after)
        w_in_grad[half] = handle["srcs"]
        sent.append(("w_in%d" % i, ["w_in#%d" % i], handle))
        return tok

    def send(names, grads, x_after):
        if names == ("w_in_a",):
            w_in_grad["a"] = list(grads)
            return _tie(x_after, send_w_in(0, x_after), "tie_w_in0")
        if names == ("w_in_b",):
            w_in_grad["b"] = list(grads)
            return x_after
        arrs, leaves = [], []
        for nm, g in zip(names, grads):
            if nm in ("w_gate_t", "w_up_t"):
                arrs.append(rowshard(g))
                leaves.append({"w_gate_t": "w_ff_gate", "w_up_t": "w_ff_up"}[nm])
            elif nm == "w_down":
                arrs.append(rowshard(g))
                leaves.append("w_ff_down")
            elif nm == "w_out":
                arrs.append(rowshard(g[GOFF:GOFF + D]))
                leaves.append(nm)
            else:
                arrs.append(shard(g[:, GOFF:GOFF + D]))
                leaves.append(nm)
        handle, tok = _split_start(_view_block, [a.shape for a in arrs], arrs, "grads_%s_start" % names[0], x_after)
        sent.append((names[0], leaves, handle))
        return _tie(x_after, tok, "tie_" + names[0])

    r = _local_step(z, loss_target[0], modc, modx, norms, onw, hg_lb_full, wgk_k, bgk_k,
                    get_w_in, get_mix, get_ffn, send)
    grad_x = r["grad_x"][None]

    sm_pre, sm_mid, sm_fin = r["sm_pre"], r["sm_mid"], r["sm_final"]
    dmodc = jnp.stack([sm_pre[0], sm_pre[2], sm_mid[4], sm_mid[0], sm_mid[2], sm_fin[0]]).reshape(-1)
    dmodx = jnp.stack([sm_pre[1], sm_pre[3], sm_mid[5], sm_mid[1], sm_mid[3], sm_fin[1]]).reshape(-1)
    on = r["sm_post"][0].reshape(NH, DH)
    pieces = [dmodc, dmodx, sm_pre[4], sm_mid[7], sm_mid[6], sm_fin[2], on[:NH // 2].sum(0), on[NH // 2:].sum(0),
              r["d_lb"][:2].reshape(-1), _unlayout_wgk(r["d_wgk"]).reshape(-1), r["d_bgk"][0]]
    loss_local = (0.5 / D) * jnp.sum(r["loss_vec"])
    pieces.append(jnp.concatenate([loss_local.reshape(1), jnp.zeros((DH - 1,), F32)]))
    sizes = [p.shape[0] for p in pieces]
    pack = jnp.concatenate(pieces).reshape(-1, DH)
    moms = [(m_w_in, v_w_in), (m_w_br_hg, v_w_br_hg), (m_w_br_gla, v_w_br_gla), (m_w_out, v_w_out),
            (m_w_ff_gate, v_w_ff_gate), (m_w_ff_up, v_w_ff_up), (m_w_ff_down, v_w_ff_down)]
    names = ["w_in", "w_br_hg", "w_br_gla", "w_out", "w_ff_gate", "w_ff_up", "w_ff_down"]
    wmv = {nm: (w, m, v) for nm, w, (m, v) in zip(names, big, moms)}
    res = {}

    def update(nm):
        w, m, v = wmv[nm]
        if nm in ("w_ff_gate", "w_ff_up"):
            outs = _adamw(recv[nm], w, tr_(m), tr_(v), "adamw_" + nm)
            res[nm] = [jnp.swapaxes(o, 0, 1)[None] for o in outs]
        else:
            res[nm] = [o[None] for o in _adamw(recv[nm], w, m[0], v[0], "adamw_" + nm)]

    small_handle, tok = _split_start(_view_whole, [(N_DEV,) + pack.shape], [pack], "small_grads_start", pack)
    tok = send_w_in(1, tok)
    recv = {}
    for first, leaves, handle in sent:
        if not first.startswith("w_in"):
            recv.update(zip(leaves, _split_wait(handle, "grads_%s_wait" % first, tok)))
    update("w_ff_gate")
    update("w_ff_up")
    pack_all, = _split_wait(small_handle, "small_grads_wait", [res["w_ff_gate"][0], res["w_ff_up"][0]])
    tot = _sum_devices(pack_all).reshape(-1)
    offs = [sum(sizes[:i]) for i in range(len(sizes))]
    part = lambda i: tot[offs[i]:offs[i] + sizes[i]]
    dmodc_t, dmodx_t = part(0), part(1)
    g_b_mod = (dmodc_t + dmodx_t)[None]
    g_norms = [part(i)[None] for i in (2, 3, 4, 5)]
    g_hg_on, g_gla_on = part(6)[None], part(7)[None]
    lb0 = lax.dynamic_slice(part(8).reshape(2, HW), (0, me * (HW // N_DEV)), (2, HW // N_DEV))
    g_hg_lb = jnp.stack([lb0, -lb0])
    g_wgk = lax.dynamic_slice(part(9).reshape(2, 16, HW), (0, 0, me * (HW // N_DEV)), (2, 16, HW // N_DEV))[None]
    g_bgk = lax.dynamic_slice(part(10).reshape(2, HW), (0, me * (HW // N_DEV)), (2, HW // N_DEV))[None]
    loss = part(11)[0]

    dmx_all = pack_all.reshape(N_DEV, -1)[:, sizes[0]:sizes[0] + sizes[1]]
    d9 = jnp.concatenate([lax.dynamic_slice(dmodc_t[None], (0, me * n_mod), (1, n_mod)),
                          lax.dynamic_slice(dmx_all, (0, me * n_mod), (N_DEV, n_mod)),
                          jnp.zeros((16 - 1 - N_DEV, n_mod), F32)], axis=0)
    g_w_mod, dcc_part = _mod_bwd(a9, d9, w_mod[0])
    cctx_handle, tok = _split_start(_view_whole, [(N_DEV,) + dcc_part.shape], [dcc_part], "c_ctx_start", dcc_part)
    tok = send_w_in(2, tok)
    recv["w_ff_down"] = _tie(recv["w_ff_down"], tok, "tie_down")
    update("w_ff_down")
    res["w_mod"] = [o[None] for o in _adamw(g_w_mod[None], w_mod[0], m_w_mod[0], v_w_mod[0], "adamw_w_mod")]
    for nm in ("w_out", "w_br_hg", "w_br_gla"):
        update(nm)
    dcc_all, = _split_wait(cctx_handle, "c_ctx_wait", [res["w_ff_down"][0], res["w_mod"][0]])
    g_c_ctx = _sum_devices(dcc_all)[0]

    small = [("c_ctx", c_ctx, m_c_ctx, v_c_ctx, g_c_ctx), ("b_mod", b_mod, m_b_mod, v_b_mod, g_b_mod),
             ("norm_pre1", norm_pre1, m_norm_pre1, v_norm_pre1, g_norms[0]),
             ("norm_post1", norm_post1, m_norm_post1, v_norm_post1, g_norms[1]),
             ("norm_pre2", norm_pre2, m_norm_pre2, v_norm_pre2, g_norms[2]),
             ("norm_post2", norm_post2, m_norm_post2, v_norm_post2, g_norms[3]),
             ("hg_lb", hg_lb, m_hg_lb, v_hg_lb, g_hg_lb), ("hg_onorm", hg_onorm, m_hg_onorm, v_hg_onorm, g_hg_on),
             ("gla_w_gk", gla_w_gk, m_gla_w_gk, v_gla_w_gk, g_wgk), ("gla_b_gk", gla_b_gk, m_gla_b_gk, v_gla_b_gk, g_bgk),
             ("gla_onorm", gla_onorm, m_gla_onorm, v_gla_onorm, g_gla_on)]
    flat = lambda k: jnp.concatenate([s[k].reshape(-1) for s in small]).reshape(-1, DH)
    outs = _adamw(flat(4)[None], flat(1), flat(2), flat(3), "adamw_small")
    off = 0
    for nm, w, _, _, _ in small:
        res[nm] = [o.reshape(-1)[off:off + w.size].reshape(w.shape) for o in outs]
        off += w.size

    done = [res[nm][0] for nm in names[1:]] + [res["w_mod"][0], outs[0]]
    sums = []
    for i, (first, leaves, handle) in enumerate(s for s in sent if s[0].startswith("w_in")):
        half = W_IN_GRAD_CHUNKS[i][0]
        land, = _split_wait(handle, "grads_%s_wait" % first, done, srcs=w_in_grad[half])
        w_in_grad[half] = handle["srcs"]
        sums.append(_sum_windows(land, "sum_windows%d" % i))
    major = lambda a: jnp.transpose(a, (2, 0, 1))
    outs = _adamw_rows3(jnp.concatenate(sums, axis=1), major(w_in), major(m_w_in), major(v_w_in), "adamw_w_in")
    res["w_in"] = [jnp.transpose(o, (1, 2, 0)) for o in outs]

    order = ["c_ctx", "w_mod", "b_mod", "norm_pre1", "norm_post1", "norm_pre2", "norm_post2", "w_in", "hg_lb",
             "hg_onorm", "gla_w_gk", "gla_b_gk", "gla_onorm", "w_br_hg", "w_br_gla", "w_out", "w_ff_gate", "w_ff_up",
             "w_ff_down"]
    return (loss, grad_x, *[res[n][k] for k in range(4) for n in order])
```

```python
import functools

import jax
import jax.numpy as jnp
from jax import lax
from jax.experimental import pallas as pl
from jax.experimental.pallas import tpu as pltpu

F32 = jnp.float32
BF16 = jnp.bfloat16
HI = lax.Precision.HIGHEST

N_DEV = 8
D = 1024
CTX = 256
HW = 512
DH = 128
NH = 8
D_FF = 2816
EPS = 1e-6
GLA_NORM = 16.0
CHUNK = 64
TR = 256
NCT = CTX // TR
W_IN_COLS = 7168
MAIN0 = 0
LR0 = 4608
GW = 1152
GOFF = 32
GATE_HG0 = LR0
GATE_GLA0 = LR0 + D
LEVELS = (32, 16, 8)
EXP_CLAMP = 80.0
VMEM_LIMIT = 48 * 1024 * 1024

ADAM_LR, ADAM_B1, ADAM_B2, ADAM_EPS, ADAM_WD, ADAM_STEP = 0.001, 0.9, 0.999, 1e-08, 0.01, 10


def _cp(*sem):
    return pltpu.CompilerParams(dimension_semantics=sem, vmem_limit_bytes=VMEM_LIMIT)


def _sig(x):
    return jax.nn.sigmoid(x)


def _silu(x):
    return x * _sig(x)


def _dsilu(x):
    s = _sig(x)
    return s * (1.0 + x * (1.0 - s))


def _rstd(x):
    return lax.rsqrt(jnp.mean(x * x, axis=-1, keepdims=True) + EPS)


def _rms_bwd(a, y, r):
    return r * (a - y * (r * r) * jnp.mean(a * y, axis=-1, keepdims=True))


def _colsum(x):
    return jnp.sum(x, axis=0, keepdims=True)


def _dot(a, b, dims, precision=None):
    return lax.dot_general(a, b, (dims, ((), ())), preferred_element_type=F32, precision=precision)


NN = ((1,), (0,))
NT = ((1,), (1,))
TN = ((0,), (0,))

SCAN_HEADS_FWD = 4
SCAN_HEADS_BWD = 4


def _split_dot(m, x):
    mb = m.astype(BF16)
    x1 = x.astype(BF16)
    r1 = x - x1.astype(F32)
    x2 = r1.astype(BF16)
    x3 = (r1 - x2.astype(F32)).astype(BF16)
    return _dot(mb, x1, NN) + _dot(mb, x2, NN) + _dot(mb, x3, NN)


def _matmul(a, b, dims, out_dtype, name, tm, tn, tk, a_off=0, m_out=None):
    a_pair = isinstance(a, (tuple, list))
    as_ = list(a) if a_pair else [a]
    a = as_[0]
    pair = isinstance(b, (tuple, list))
    bs = list(b) if pair else [b]
    b1 = bs[0]
    rows = b1.shape[0] * len(bs)
    half = None
    if dims == NN:
        m, k, n = a.shape[0], rows, b1.shape[1]
        a_spec = pl.BlockSpec((tm, tk), lambda i, j, kk: (i, kk + a_off))
        half = b1.shape[0] // tk
        if a_pair:
            assert pair and a.shape[1] == b1.shape[0] and a_off == 0
            a_spec = [pl.BlockSpec((tm, tk), lambda i, j, kk: (i, jnp.minimum(kk, half - 1))),
                      pl.BlockSpec((tm, tk), lambda i, j, kk: (i, jnp.maximum(kk - half, 0)))]
        b_maps = [lambda i, j, kk: (kk, j)] if not pair else [
            lambda i, j, kk: (jnp.minimum(kk, half - 1), j), lambda i, j, kk: (jnp.maximum(kk - half, 0), j)]
        b_specs = [pl.BlockSpec((tk, tn), f) for f in b_maps]
        axis = 2
    elif dims == NT:
        m, k, n = a.shape[0], b1.shape[1], rows
        a_spec = pl.BlockSpec((tm, tk), lambda i, j, kk: (i, kk + a_off))
        half = b1.shape[0] // tn
        b_maps = [lambda i, j, kk: (j, kk)] if not pair else [
            lambda i, j, kk: (jnp.minimum(j, half - 1), kk), lambda i, j, kk: (jnp.maximum(j - half, 0), kk)]
        b_specs = [pl.BlockSpec((tn, tk), f) for f in b_maps]
        axis = 1
    else:
        assert not pair
        m, k = (a.shape[1] if m_out is None else m_out), a.shape[0]
        n = b1.shape[1]
        a_spec = pl.BlockSpec((tk, tm), lambda i, j, kk: (kk, i + a_off))
        b_specs = [pl.BlockSpec((tk, tn), lambda i, j, kk: (kk, j))]
    assert m % tm == 0 and n % tn == 0 and k % tk == 0, (name, m, n, k, tm, tn, tk)
    nk = k // tk
    nb = len(bs)
    na = len(as_)
    assert na == 1 or dims == NN

    def body(*refs):
        a_refs, refs = refs[:na], refs[na:]
        o_ref = refs[nb]
        if pair:
            bv = jnp.where(pl.program_id(axis) < half, refs[0][...], refs[1][...])
        else:
            bv = refs[0][...]
        av = a_refs[0][...] if na == 1 else jnp.where(pl.program_id(2) < half, a_refs[0][...], a_refs[1][...])
        part = _dot(av, bv, dims)
        if nk == 1:
            o_ref[...] = part.astype(o_ref.dtype)
            return
        acc_ref = refs[nb + 1]
        kk = pl.program_id(2)

        @pl.when(kk == 0)
        def _():
            acc_ref[...] = part

        @pl.when(kk > 0)
        def _():
            acc_ref[...] += part

        @pl.when(kk == nk - 1)
        def _():
            o_ref[...] = acc_ref[...].astype(o_ref.dtype)

    return pl.pallas_call(
        body,
        name=name,
        grid=(m // tm, n // tn, nk),
        in_specs=(a_spec if a_pair else [a_spec]) + b_specs,
        out_specs=pl.BlockSpec((tm, tn), lambda i, j, kk: (i, j)),
        out_shape=jax.ShapeDtypeStruct((m, n), out_dtype),
        scratch_shapes=[] if nk == 1 else [pltpu.VMEM((tm, tn), F32)],
        compiler_params=_cp("parallel", "parallel", "arbitrary"),
    )(*as_, *bs)


def _mm_gu_act(h, w_gate_t, w_up_t, name, tm):
    t = h.shape[0]
    tn = D_FF // 2

    def body(a_ref, bg_ref, bu_ref, u_ref, v_ref, act_ref):
        a = a_ref[...]
        u = _dot(a, bg_ref[...], NT)
        v = _dot(a, bu_ref[...], NT)
        u_ref[...] = u.astype(BF16)
        v_ref[...] = v.astype(BF16)
        act_ref[...] = (_silu(u) * v).astype(BF16)

    wspec = pl.BlockSpec((tn, D), lambda i, j: (j, 0))
    ospec = pl.BlockSpec((tm, tn), lambda i, j: (i, j))
    out = jax.ShapeDtypeStruct((t, D_FF), BF16)
    return pl.pallas_call(
        body, name=name, grid=(t // tm, D_FF // tn),
        in_specs=[pl.BlockSpec((tm, D), lambda i, j: (i, 0)), wspec, wspec],
        out_specs=[ospec] * 3, out_shape=[out] * 3,
        compiler_params=_cp("parallel", "parallel"),
    )(h, w_gate_t, w_up_t)


def _mm_down_dx_act(dy, w_down, u, v, name, tm):
    t = dy.shape[0]
    tn = D_FF // 2

    def body(a_ref, b_ref, u_ref, v_ref, du_ref, dv_ref):
        dact = _dot(a_ref[...], b_ref[...], NT)
        u = u_ref[...].astype(F32)
        du_ref[...] = (dact * v_ref[...].astype(F32) * _dsilu(u)).astype(BF16)
        dv_ref[...] = (dact * _silu(u)).astype(BF16)

    ospec = pl.BlockSpec((tm, tn), lambda i, j: (i, j))
    out = jax.ShapeDtypeStruct((t, D_FF), BF16)
    return pl.pallas_call(
        body, name=name, grid=(t // tm, D_FF // tn),
        in_specs=[pl.BlockSpec((tm, D), lambda i, j: (i, 0)), pl.BlockSpec((tn, D), lambda i, j: (j, 0)), ospec, ospec],
        out_specs=[ospec] * 2, out_shape=[out] * 2,
        compiler_params=_cp("parallel", "parallel"),
    )(dy, w_down, u, v)


def _row(c):
    return pl.BlockSpec((TR, c), lambda i: (i, 0))


def _rowcol(width, cb):
    return pl.BlockSpec((TR, width), lambda i: (i, cb))


def _full(shape):
    return pl.BlockSpec(shape, lambda i: (0,) * len(shape))


def _mod_row(mc_ref, mx_ref, k, is_ctx):
    return jnp.where(is_ctx, mc_ref[k:k + 1, :], mx_ref[k:k + 1, :])


def _z_specs():
    return [pl.BlockSpec((TR, D), lambda i: (jnp.minimum(i, NCT - 1), 0)),
            pl.BlockSpec((TR, D), lambda i: (jnp.maximum(i - NCT, 0), 0))]


def _z_tile(c_ref, x_ref, is_ctx):
    return jnp.where(is_ctx, c_ref[...], x_ref[...])


def _acc_row(ref, k, val):
    ref[k:k + 1, :] += val


def _acc_mod(ref, k, is_ctx, val):
    zero = jnp.zeros_like(val)
    ref[k:k + 1, :] += jnp.where(is_ctx, val, zero)
    ref[k + 1:k + 2, :] += jnp.where(is_ctx, zero, val)


def _prenorm(z, nw, modc, modx, i_shift, i_scale, name):
    t = z[0].shape[0] + z[1].shape[0]

    def body(zc_ref, zx_ref, nw_ref, mc_ref, mx_ref, h_ref):
        is_ctx = pl.program_id(0) < NCT
        x = _z_tile(zc_ref, zx_ref, is_ctx)
        n = x * _rstd(x) * nw_ref[...]
        h = n * (1.0 + _mod_row(mc_ref, mx_ref, i_scale, is_ctx)) + _mod_row(mc_ref, mx_ref, i_shift, is_ctx)
        h_ref[...] = h.astype(BF16)

    return pl.pallas_call(
        body, name=name, grid=(t // TR,),
        in_specs=_z_specs() + [_full((1, D)), _full((8, D)), _full((8, D))],
        out_specs=_row(D),
        out_shape=jax.ShapeDtypeStruct((t, D), BF16),
        compiler_params=_cp("parallel"),
    )(*z, nw, modc, modx)


def _hg_lb(lb_ref, d):
    a0 = lb_ref[0, d:d + 1, :]
    a1 = lb_ref[1, d:d + 1, :]
    mx = jnp.maximum(a0, a1)
    e0 = jnp.exp(a0 - mx)
    e1 = jnp.exp(a1 - mx)
    return e0 / (e0 + e1)


def _log_sigmoid(x):
    return jnp.minimum(x, 0.0) - jnp.log(1.0 + jnp.exp(-jnp.abs(x)))


def _gates_fwd(p, hg_lb, wgk, bgk):
    t = p.shape[0]
    seg = lambda j: _rowcol(HW, MAIN0 // HW + j)

    def body(hq_ref, hi_ref, hf_ref, hb_ref, gq_ref, gk_ref, gv_ref, lr_ref, lb_ref, wgk_ref, bgk_ref,
             q_ref, v_ref, kf_ref, kb_ref, gf_ref, gb_ref):
        q_ref[:, :HW] = _silu(hq_ref[...].astype(F32)).astype(BF16)
        q_ref[:, HW:] = (gq_ref[...].astype(F32) * (DH ** -0.5)).astype(BF16)
        v_ref[:, :HW] = hi_ref[...]
        v_ref[:, HW:] = gv_ref[...]
        xg = _dot(lr_ref[...].astype(BF16), wgk_ref[...], NN) + bgk_ref[...]
        for d, (raw_ref, k_ref, g_ref) in enumerate(((hf_ref, kf_ref, gf_ref), (hb_ref, kb_ref, gb_ref))):
            lbd = _hg_lb(lb_ref, d)
            f = lbd + (1.0 - lbd) * _sig(raw_ref[...].astype(F32))
            k_ref[:, :HW] = (1.0 - f).astype(BF16)
            k_ref[:, HW:] = gk_ref[...]
            g_ref[:, :HW] = jnp.log(f)
            g_ref[:, HW:] = _log_sigmoid(xg[:, d * HW:(d + 1) * HW]) * (1.0 / GLA_NORM)

    out = jax.ShapeDtypeStruct((t, D), F32)
    outb = jax.ShapeDtypeStruct((t, D), BF16)
    return pl.pallas_call(
        body, name="gates_fwd", grid=(t // TR,),
        in_specs=[seg(0), seg(1), seg(2), seg(3), seg(5), seg(6), seg(7), _rowcol(DH, LR0 // DH),
                  _full((2, 2, HW)), _full((DH, D)), _full((1, D))],
        out_specs=[_row(D)] * 6,
        out_shape=[outb] * 4 + [out] * 2,
        compiler_params=_cp("parallel"),
    )(p, p, p, p, p, p, p, p, hg_lb, wgk, bgk)


def _post_fwd(o_fw, o_bw, p, onw):
    t = o_fw.shape[0]

    def body(of_ref, ob_ref, g1_ref, g2_ref, w_ref, y_ref):
        for h in range(NH):
            sl = slice(h * DH, (h + 1) * DH)
            o = of_ref[:, sl] + ob_ref[:, sl]
            g_ref = g1_ref if h < NH // 2 else g2_ref
            gs = slice((h % (NH // 2)) * DH, (h % (NH // 2) + 1) * DH)
            n = o * _rstd(o) * w_ref[:, sl]
            y_ref[:, sl] = (n * _silu(g_ref[:, gs].astype(F32))).astype(BF16)

    return pl.pallas_call(
        body, name="post_fwd", grid=(t // TR,),
        in_specs=[_row(D), _row(D), _rowcol(HW, MAIN0 // HW + 4), _rowcol(HW, MAIN0 // HW + 8), _full((1, D))],
        out_specs=_row(D),
        out_shape=jax.ShapeDtypeStruct((t, D), BF16),
        compiler_params=_cp("parallel"),
    )(o_fw, o_bw, p, p, onw)


def _gate_window_specs(col0):
    return [_rowcol(HW, col0 // HW), _rowcol(HW, col0 // HW + 1), _rowcol(DH, (col0 + 2 * HW) // DH)]


def _gate_window(refs):
    return jnp.concatenate([r[...].astype(F32) for r in refs], axis=1)


def _merge_fwd(p, u1, u2):
    t = p.shape[0]

    def body(a0, a1, a2, b0, b1, b2, u1_ref, u2_ref, m_ref):
        f = lambda r: r[...].astype(F32)
        m_ref[...] = (_sig(_gate_window((a0, a1, a2))) * f(u1_ref)
                      + _sig(_gate_window((b0, b1, b2))) * f(u2_ref)).astype(BF16)

    return pl.pallas_call(
        body, name="merge_fwd", grid=(t // TR,),
        in_specs=_gate_window_specs(GATE_HG0) + _gate_window_specs(GATE_GLA0) + [_row(GW), _row(GW)],
        out_specs=_row(GW),
        out_shape=jax.ShapeDtypeStruct((t, GW), BF16),
        compiler_params=_cp("parallel"),
    )(p, p, p, p, p, p, u1, u2)


def _mid_fwd(z, y1, nw_post, nw_pre, modc, modx):
    t = y1.shape[0]

    def body(zc_ref, zx_ref, y_ref, wpo_ref, wpr_ref, mc_ref, mx_ref, z1_ref, h_ref):
        is_ctx = pl.program_id(0) < NCT
        y = y_ref[...].astype(F32)
        z1 = _z_tile(zc_ref, zx_ref, is_ctx) + _mod_row(mc_ref, mx_ref, 2, is_ctx) * (y * _rstd(y) * wpo_ref[...])
        z1_ref[...] = z1
        n = z1 * _rstd(z1) * wpr_ref[...]
        h = n * (1.0 + _mod_row(mc_ref, mx_ref, 4, is_ctx)) + _mod_row(mc_ref, mx_ref, 3, is_ctx)
        h_ref[...] = h.astype(BF16)

    return pl.pallas_call(
        body, name="mid_fwd", grid=(t // TR,),
        in_specs=_z_specs() + [_row(D), _full((1, D)), _full((1, D)), _full((8, D)), _full((8, D))],
        out_specs=[_row(D), _row(D)],
        out_shape=[jax.ShapeDtypeStruct((t, D), F32), jax.ShapeDtypeStruct((t, D), BF16)],
        compiler_params=_cp("parallel"),
    )(*z, y1, nw_post, nw_pre, modc, modx)


def _final(z1, y2, target, nw, modc, modx):
    t = z1.shape[0]

    def body(z1_ref, y_ref, tg_ref, w_ref, mc_ref, mx_ref, dz_ref, dy_ref, loss_ref, sm_ref):
        i = pl.program_id(0)
        is_ctx = i < NCT

        @pl.when(i == 0)
        def _():
            loss_ref[...] = jnp.zeros_like(loss_ref)
            sm_ref[...] = jnp.zeros_like(sm_ref)

        g = _mod_row(mc_ref, mx_ref, 5, is_ctx)
        y = y_ref[...].astype(F32)
        r = _rstd(y)
        w = w_ref[...]
        yr = y * r
        n = yr * w
        e = z1_ref[...] + g * n - tg_ref[...]
        lat = jnp.where(is_ctx, 0.0, 1.0)
        loss_ref[...] += lat * _colsum(e * e)
        dz = e * (lat / D)
        dz_ref[...] = dz
        _acc_mod(sm_ref, 0, is_ctx, _colsum(dz * n))
        dn = dz * g
        _acc_row(sm_ref, 2, _colsum(dn * yr))
        dy_ref[...] = _rms_bwd(dn * w, y, r).astype(BF16)

    return pl.pallas_call(
        body, name="final", grid=(t // TR,),
        in_specs=[_row(D), _row(D), pl.BlockSpec((TR, D), lambda i: (jnp.maximum(i - NCT, 0), 0)),
                  _full((1, D)), _full((8, D)), _full((8, D))],
        out_specs=[_row(D), _row(D), _full((1, D)), _full((8, D))],
        out_shape=[jax.ShapeDtypeStruct((t, D), F32), jax.ShapeDtypeStruct((t, D), BF16),
                   jax.ShapeDtypeStruct((1, D), F32), jax.ShapeDtypeStruct((8, D), F32)],
        compiler_params=_cp("arbitrary"),
    )(z1, y2, target, nw, modc, modx)


def _mid_bwd(dh2, dz, z1, y1, nw_post, nw_pre, modc, modx):
    t = z1.shape[0]

    def body(dh_ref, dz_ref, z1_ref, y_ref, wpo_ref, wpr_ref, mc_ref, mx_ref, dzo_ref, dy_ref, sm_ref):
        i = pl.program_id(0)
        is_ctx = i < NCT

        @pl.when(i == 0)
        def _():
            sm_ref[...] = jnp.zeros_like(sm_ref)

        dh = dh_ref[...].astype(F32)
        z1 = z1_ref[...]
        r = _rstd(z1)
        zr = z1 * r
        wpr = wpr_ref[...]
        n = zr * wpr
        _acc_mod(sm_ref, 0, is_ctx, _colsum(dh))
        _acc_mod(sm_ref, 2, is_ctx, _colsum(dh * n))
        dn = dh * (1.0 + _mod_row(mc_ref, mx_ref, 4, is_ctx))
        _acc_row(sm_ref, 6, _colsum(dn * zr))
        dz1 = dz_ref[...] + _rms_bwd(dn * wpr, z1, r)
        dzo_ref[...] = dz1
        y = y_ref[...].astype(F32)
        r1 = _rstd(y)
        yr = y * r1
        wpo = wpo_ref[...]
        g = _mod_row(mc_ref, mx_ref, 2, is_ctx)
        _acc_mod(sm_ref, 4, is_ctx, _colsum(dz1 * (yr * wpo)))
        dn1 = dz1 * g
        _acc_row(sm_ref, 7, _colsum(dn1 * yr))
        dy_ref[...] = _rms_bwd(dn1 * wpo, y, r1).astype(BF16)

    return pl.pallas_call(
        body, name="mid_bwd", grid=(t // TR,),
        in_specs=[_row(D)] * 4 + [_full((1, D)), _full((1, D)), _full((8, D)), _full((8, D))],
        out_specs=[_row(D), _row(D), _full((8, D))],
        out_shape=[jax.ShapeDtypeStruct((t, D), F32), jax.ShapeDtypeStruct((t, D), BF16),
                   jax.ShapeDtypeStruct((8, D), F32)],
        compiler_params=_cp("arbitrary"),
    )(dh2, dz, z1, y1, nw_post, nw_pre, modc, modx)


def _pre_bwd(dh1, dz, z, nw, modc, modx):
    t = dh1.shape[0]

    def body(dh_ref, dz_ref, zc_ref, zx_ref, w_ref, mc_ref, mx_ref, dzo_ref, sm_ref):
        i = pl.program_id(0)
        is_ctx = i < NCT

        @pl.when(i == 0)
        def _():
            sm_ref[...] = jnp.zeros_like(sm_ref)

        dh = dh_ref[...].astype(F32)
        x = _z_tile(zc_ref, zx_ref, is_ctx)
        r = _rstd(x)
        xr = x * r
        w = w_ref[...]
        _acc_mod(sm_ref, 0, is_ctx, _colsum(dh))
        _acc_mod(sm_ref, 2, is_ctx, _colsum(dh * (xr * w)))
        dn = dh * (1.0 + _mod_row(mc_ref, mx_ref, 1, is_ctx))
        _acc_row(sm_ref, 4, _colsum(dn * xr))
        dzo_ref[...] = dz_ref[...] + _rms_bwd(dn * w, x, r)

    return pl.pallas_call(
        body, name="pre_bwd", grid=(t // TR,),
        in_specs=[_row(D)] * 2 + _z_specs() + [_full((1, D)), _full((8, D)), _full((8, D))],
        out_specs=[pl.BlockSpec((TR, D), lambda i: (jnp.maximum(i - NCT, 0), 0)), _full((8, D))],
        out_shape=[jax.ShapeDtypeStruct((t - CTX, D), F32), jax.ShapeDtypeStruct((8, D), F32)],
        compiler_params=_cp("arbitrary"),
    )(dh1, dz, *z, nw, modc, modx)


def _merge_bwd(dm, p, u1, u2):
    t = dm.shape[0]

    def body(dm_ref, a0, a1, a2, b0, b1, b2, u1_ref, u2_ref, du1_ref, du2_ref, dg_ref):
        dm_ = dm_ref[...].astype(F32)
        s1 = _sig(_gate_window((a0, a1, a2)))
        s2 = _sig(_gate_window((b0, b1, b2)))
        du1_ref[...] = (dm_ * s1).astype(BF16)
        du2_ref[...] = (dm_ * s2).astype(BF16)
        dg_ref[:, :GW] = (dm_ * u1_ref[...].astype(F32) * s1 * (1.0 - s1)).astype(BF16)
        dg_ref[:, GW:] = (dm_ * u2_ref[...].astype(F32) * s2 * (1.0 - s2)).astype(BF16)

    return pl.pallas_call(
        body, name="merge_bwd", grid=(t // TR,),
        in_specs=[_row(GW)] + _gate_window_specs(GATE_HG0) + _gate_window_specs(GATE_GLA0) + [_row(GW), _row(GW)],
        out_specs=[_row(GW), _row(GW), _row(2 * GW)],
        out_shape=[jax.ShapeDtypeStruct((t, GW), BF16), jax.ShapeDtypeStruct((t, GW), BF16),
                   jax.ShapeDtypeStruct((t, 2 * GW), BF16)],
        compiler_params=_cp("parallel"),
    )(dm, p, p, p, p, p, p, u1, u2)


def _post_bwd(dy_hg, dy_gla, o_fw, o_bw, p, onw):
    t = o_fw.shape[0]

    def body(d1_ref, d2_ref, of_ref, ob_ref, g1_ref, g2_ref, w_ref, do_ref, dg_ref, sm_ref):
        @pl.when(pl.program_id(0) == 0)
        def _():
            sm_ref[...] = jnp.zeros_like(sm_ref)

        for h in range(NH):
            sl = slice(h * DH, (h + 1) * DH)
            gs = slice((h % (NH // 2)) * DH, (h % (NH // 2) + 1) * DH)
            g_ref, d_ref = (g1_ref, d1_ref) if h < NH // 2 else (g2_ref, d2_ref)
            o = of_ref[:, sl] + ob_ref[:, sl]
            r = _rstd(o)
            orr = o * r
            w = w_ref[:, sl]
            gt = g_ref[:, gs].astype(F32)
            dy = d_ref[:, gs].astype(F32)
            dg_ref[:, sl] = (dy * (orr * w) * _dsilu(gt)).astype(BF16)
            dn = dy * _silu(gt)
            sm_ref[0:1, sl] += _colsum(dn * orr)
            do_ref[:, sl] = _rms_bwd(dn * w, o, r)

    return pl.pallas_call(
        body, name="post_bwd", grid=(t // TR,),
        in_specs=[_row(HW), _row(HW), _row(D), _row(D), _rowcol(HW, MAIN0 // HW + 4), _rowcol(HW, MAIN0 // HW + 8),
                  _full((1, D))],
        out_specs=[_row(D), _row(D), _full((8, D))],
        out_shape=[jax.ShapeDtypeStruct((t, D), F32), jax.ShapeDtypeStruct((t, D), BF16),
                   jax.ShapeDtypeStruct((8, D), F32)],
        compiler_params=_cp("arbitrary"),
    )(dy_hg, dy_gla, o_fw, o_bw, p, p, onw)


def _gates_bwd(p, hg_lb, wgk, bgk, dgm, dgo, dq_f, dq_b, dv_f, dv_b, dk_f, dk_b, dg_f, dg_b):
    t = p.shape[0]
    seg = lambda j: _rowcol(HW, MAIN0 // HW + j)

    def body(hq_ref, hf_ref, hb_ref, lr_ref, lb_ref, wgk_ref, bgk_ref, dgm_ref, dgo_ref,
             dqf_ref, dqb_ref, dvf_ref, dvb_ref, dkf_ref, dkb_ref, dgf_ref, dgb_ref,
             dp_ref, dlb_ref, dw_ref, db_ref):
        @pl.when(pl.program_id(0) == 0)
        def _():
            dlb_ref[...] = jnp.zeros_like(dlb_ref)
            dw_ref[...] = jnp.zeros_like(dw_ref)
            db_ref[...] = jnp.zeros_like(db_ref)

        c0 = MAIN0

        def put(j, val):
            dp_ref[:, c0 + j * HW:c0 + (j + 1) * HW] = val.astype(BF16)

        dq = dqf_ref[...].astype(F32) + dqb_ref[...].astype(F32)
        dv = dvf_ref[...].astype(F32) + dvb_ref[...].astype(F32)
        put(0, dq[:, :HW] * _dsilu(hq_ref[...].astype(F32)))
        put(1, dv[:, :HW])
        put(5, dq[:, HW:] * (DH ** -0.5))
        put(7, dv[:, HW:])
        put(6, dkf_ref[:, HW:].astype(F32) + dkb_ref[:, HW:].astype(F32))
        dp_ref[:, c0 + 4 * HW:c0 + 5 * HW] = dgo_ref[:, :HW]
        dp_ref[:, c0 + 8 * HW:c0 + 9 * HW] = dgo_ref[:, HW:]
        lr = lr_ref[...].astype(BF16)
        xg = _dot(lr, wgk_ref[...], NN) + bgk_ref[...]
        dxg = []
        for d, (raw_ref, dk_ref, dg_ref) in enumerate(((hf_ref, dkf_ref, dgf_ref), (hb_ref, dkb_ref, dgb_ref))):
            lbd = _hg_lb(lb_ref, d)
            s = _sig(raw_ref[...].astype(F32))
            f = lbd + (1.0 - lbd) * s
            df = dg_ref[:, :HW] / f - dk_ref[:, :HW].astype(F32)
            put(2 + d, df * (1.0 - lbd) * s * (1.0 - s))
            dlb_ref[d:d + 1, :] += _colsum(df * (1.0 - s)) * (lbd * (1.0 - lbd))
            dxg.append(dg_ref[:, HW:] * (1.0 / GLA_NORM) * _sig(-xg[:, d * HW:(d + 1) * HW]))
        dxg = jnp.concatenate(dxg, axis=1)
        db_ref[0:1, :] += _colsum(dxg)
        dxg_b = dxg.astype(BF16)
        dw_ref[...] += _dot(lr, dxg_b, TN)
        dlr = _dot(dxg_b, wgk_ref[...], NT)
        dp_ref[:, LR0:LR0 + DH] = (dlr + dgm_ref[:, :DH].astype(F32)).astype(BF16)
        dp_ref[:, LR0 + DH:GATE_GLA0] = dgm_ref[:, DH:D]
        dp_ref[:, GATE_GLA0:GATE_GLA0 + DH] = dgm_ref[:, D:GW] + dgm_ref[:, GW:GW + DH]
        dp_ref[:, GATE_GLA0 + DH:GATE_GLA0 + GW] = dgm_ref[:, GW + DH:]
        dp_ref[:, GATE_GLA0 + GW:] = jnp.zeros((TR, W_IN_COLS - GATE_GLA0 - GW), BF16)

    return pl.pallas_call(
        body, name="gates_bwd", grid=(t // TR,),
        in_specs=[seg(0), seg(2), seg(3), _rowcol(DH, LR0 // DH), _full((2, 2, HW)), _full((DH, D)), _full((1, D)),
                  _row(2 * GW), _row(D)] + [_row(D)] * 8,
        out_specs=[_row(W_IN_COLS), _full((8, HW)), _full((DH, D)), _full((8, D))],
        out_shape=[jax.ShapeDtypeStruct((t, W_IN_COLS), BF16), jax.ShapeDtypeStruct((8, HW), F32),
                   jax.ShapeDtypeStruct((DH, D), F32), jax.ShapeDtypeStruct((8, D), F32)],
        compiler_params=_cp("arbitrary"),
    )(p, p, p, p, hg_lb, wgk, bgk, dgm, dgo, dq_f, dq_b, dv_f, dv_b, dk_f, dk_b, dg_f, dg_b)


def _scan_consts(rev):
    r = lax.broadcasted_iota(jnp.int32, (CHUNK, CHUNK), 0)
    u = lax.broadcasted_iota(jnp.int32, (CHUNK, CHUNK), 1)
    rp = lax.broadcasted_iota(jnp.int32, (CHUNK, 1), 0)
    if rev:
        r, u, rp = CHUNK - 1 - r, CHUNK - 1 - u, CHUNK - 1 - rp
    tri = jnp.where(u <= r, 1.0, 0.0).astype(F32)
    tri_t = jnp.where(r <= u, 1.0, 0.0).astype(F32)
    lv = []
    for b in LEVELS:
        sh = b.bit_length() - 1
        pair = ((r >> sh) == (u >> sh) + 1) & (((u >> sh) & 1) == 0)
        pair_t = ((u >> sh) == (r >> sh) + 1) & (((r >> sh) & 1) == 0)
        tside = ((rp >> sh) & 1) == 1
        lv.append((pair, pair_t, tside, jnp.where(tside, 1.0, -1.0).astype(F32)))
    bd = LEVELS[-1].bit_length() - 1
    diag = ((r >> bd) == (u >> bd)) & (u <= r)
    diag_t = ((r >> bd) == (u >> bd)) & (r <= u)
    return tri, tri_t, lv, diag, diag_t


def _row_of(pos, rev):
    return CHUNK - 1 - pos if rev else pos


def _chunk_terms(cum, b_scr, consts, rev):
    _, _, lv, _, _ = consts
    terms = []
    for b, (_, _, _, sgn) in zip(LEVELS, lv):
        pieces = []
        for j in range(CHUNK // (2 * b)):
            row = _row_of(2 * b * j + b - 1, rev)
            pieces.append(jnp.broadcast_to(b_scr[row:row + 1, :], (2 * b, DH)))
        if rev:
            pieces = pieces[::-1]
        bnd = pieces[0] if len(pieces) == 1 else jnp.concatenate(pieces, axis=0)
        terms.append(jnp.exp((cum - bnd) * sgn))
    b = LEVELS[-1]
    pieces = []
    for j in range(CHUNK // b):
        if j == 0:
            pieces.append(jnp.zeros((b, DH), F32))
        else:
            row = _row_of(b * j - 1, rev)
            pieces.append(jnp.broadcast_to(b_scr[row:row + 1, :], (b, DH)))
    if rev:
        pieces = pieces[::-1]
    start = jnp.concatenate(pieces, axis=0)
    wq = jnp.exp(jnp.minimum(cum - start, 0.0))
    wk = jnp.exp(jnp.minimum(start - cum, EXP_CLAMP))
    terms.append((wq, wk))
    return terms


def _run_staged(units):
    live = list(units)
    while live:
        nxt = []
        for u in live:
            try:
                next(u)
                nxt.append(u)
            except StopIteration:
                pass
        live = nxt


SCAN_TB = 256
SCAN_CB = SCAN_TB // CHUNK


def _block_order(i, ntb, rev):
    nctx = CTX // SCAN_TB
    if not rev:
        return i
    return jnp.where(i < nctx, nctx - 1 - i, ntb - 1 - (i - nctx))


def _chunk_in_block(j, rev):
    return SCAN_CB - 1 - j if rev else j


def _scan_fwd(q, k, v, g, rev):
    t = q.shape[0]
    nc = t // CHUNK
    hpb = SCAN_HEADS_FWD

    def body(q_ref, k_ref, v_ref, g_ref, o_ref, st_ref, s_scr, b_scr):
        consts = _scan_consts(rev)
        _, _, lv, diag, _ = consts
        masks = [lvl[0] for lvl in lv] + [diag]

        @pl.when(pl.program_id(1) == 0)
        def _():
            s_scr[...] = jnp.zeros_like(s_scr)

        tri = consts[0]
        state = {hh: s_scr[hh] for hh in range(hpb)}

        def unit(hh, j):
            sl = slice(hh * DH, (hh + 1) * DH)
            c = _chunk_in_block(j, rev)
            rows = slice(c * CHUNK, (c + 1) * CHUNK)
            b_ref = b_scr.at[hh * SCAN_CB + j]
            qc, kc, vc, gc = q_ref[rows, sl], k_ref[rows, sl], v_ref[rows, sl], g_ref[rows, sl]
            cum = _split_dot(tri, gc)
            b_ref[...] = cum
            yield
            terms = _chunk_terms(cum, b_ref, consts, rev)
            qf, kf = qc.astype(F32), kc.astype(F32)
            xs = [(jnp.where(tside, qf, kf) * w).astype(BF16) for w, (_, _, tside, _) in zip(terms[:-1], lv)]
            qd, kd = (qf * terms[-1][0]).astype(BF16), (kf * terms[-1][1]).astype(BF16)
            tot = _colsum(gc)
            qe = (qf * jnp.exp(cum)).astype(BF16)
            ke = (kf * jnp.exp(tot - cum)).astype(BF16)
            vb = vc.astype(BF16)
            yield
            scs = [_dot(x, x, NT) for x in xs] + [_dot(qd, kd, NT)]
            kv = _dot(vb, ke, TN)
            yield
            a = jnp.zeros((CHUNK, CHUNK), F32)
            for sc, m in zip(scs, masks):
                a = a + jnp.where(m, sc, 0.0)
            o_intra = _dot(a.astype(BF16), vb, NN)
            yield
            st = state[hh]
            st_ref[hh, c] = st
            o_ref[rows, sl] = o_intra + _dot(qe, st.astype(BF16), NT)
            state[hh] = st * jnp.exp(tot) + kv
            yield

        _run_staged([unit(hh, j) for hh in range(hpb) for j in range(SCAN_CB)])
        for hh in range(hpb):
            s_scr[hh] = state[hh]

    ntb = t // SCAN_TB
    col = pl.BlockSpec((SCAN_TB, hpb * DH), lambda h, i: (_block_order(i, ntb, rev), h))
    return pl.pallas_call(
        body, name="scan_fwd_" + ("bw" if rev else "fw"), grid=(NH // hpb, ntb),
        in_specs=[col] * 4,
        out_specs=[col, pl.BlockSpec((hpb, SCAN_CB, DH, DH), lambda h, i: (h, _block_order(i, ntb, rev), 0, 0))],
        out_shape=[jax.ShapeDtypeStruct((t, D), F32), jax.ShapeDtypeStruct((NH, nc, DH, DH), F32)],
        scratch_shapes=[pltpu.VMEM((hpb, DH, DH), F32), pltpu.VMEM((hpb * SCAN_CB, CHUNK, DH), F32)],
        compiler_params=_cp("parallel", "arbitrary"),
    )(q, k, v, g)


def _scan_bwd(q, k, v, g, do, states, rev):
    t = q.shape[0]
    nc = t // CHUNK
    hpb = SCAN_HEADS_BWD

    def body(q_ref, k_ref, v_ref, g_ref, do_ref, st_ref, dq_ref, dk_ref, dv_ref, dg_ref, ds_scr, b_scr):
        consts = _scan_consts(rev)
        _, tri_t, lv, diag, diag_t = consts
        masks = [(lvl[0], lvl[1]) for lvl in lv] + [(diag, diag_t)]
        @pl.when(pl.program_id(1) == 0)
        def _():
            ds_scr[...] = jnp.zeros_like(ds_scr)

        tri = consts[0]
        dstate = {hh: ds_scr[hh] for hh in range(hpb)}

        def unit(hh, jj):
            sl = slice(hh * DH, (hh + 1) * DH)
            c = _chunk_in_block(SCAN_CB - 1 - jj, rev)
            rows = slice(c * CHUNK, (c + 1) * CHUNK)
            b_ref = b_scr.at[hh * SCAN_CB + jj]
            qc, kc, vc, gc = q_ref[rows, sl], k_ref[rows, sl], v_ref[rows, sl], g_ref[rows, sl]
            dob = do_ref[rows, sl].astype(BF16)
            vb = vc.astype(BF16)
            cum = _split_dot(tri, gc)
            b_ref[...] = cum
            da = _dot(dob, vb, NT)
            da_t = _dot(vb, dob, NT)
            yield
            terms = _chunk_terms(cum, b_ref, consts, rev)
            qf, kf = qc.astype(F32), kc.astype(F32)
            xs = [(jnp.where(tside, qf, kf) * w).astype(BF16) for w, (_, _, tside, _) in zip(terms[:-1], lv)]
            wqd, wkd = terms[-1]
            qdb, kdb = (qf * wqd).astype(BF16), (kf * wkd).astype(BF16)
            tot = _colsum(gc)
            e_tot = jnp.exp(tot)
            e_b = jnp.exp(cum)
            e_t = jnp.exp(tot - cum)
            qeb = (qf * e_b).astype(BF16)
            keb = (kf * e_t).astype(BF16)
            dsym = [(jnp.where(m, da, 0.0) + jnp.where(m_t, da_t, 0.0)).astype(BF16) for m, m_t in masks[:-1]]
            dad = (jnp.where(diag, da, 0.0).astype(BF16), jnp.where(diag_t, da_t, 0.0).astype(BF16))
            yield
            sym = [_dot(x, x, NT) for x in xs]
            dxs = [_dot(d, x, NN) for d, x in zip(dsym, xs)]
            at_d = _dot(kdb, qdb, NT)
            dqt_d = _dot(dad[0], kdb, NN)
            dkt_d = _dot(dad[1], qdb, NN)
            qd = _dot(dob, qeb, TN)
            yield
            a_t = jnp.where(diag_t, at_d, 0.0)
            dq = dqt_d * wqd
            dk = dkt_d * wkd
            db = dqt_d * qdb.astype(F32) - dkt_d * kdb.astype(F32)
            for s, dx, x, w, (_, m_t, tside, sgn) in zip(sym, dxs, xs, terms[:-1], lv):
                a_t = a_t + jnp.where(m_t, s, 0.0)
                dxw = dx * w
                dq = dq + jnp.where(tside, dxw, 0.0)
                dk = dk + jnp.where(tside, 0.0, dxw)
                db = db + (dx * x.astype(F32)) * sgn
            dv_intra = _dot(a_t.astype(BF16), dob, NN)
            st = st_ref[hh, c]
            stb = st.astype(BF16)
            dqe = _dot(dob, stb, NN)
            yield
            dst = dstate[hh]
            dstb = dst.astype(BF16)
            dstate[hh] = dst * e_tot + qd
            dv_ref[rows, sl] = (dv_intra + _dot(keb, dstb, NT)).astype(BF16)
            dke = _dot(vb, dstb, NN)
            yield
            qe = qeb.astype(F32)
            ke = keb.astype(F32)
            dq_ref[rows, sl] = (dq + dqe * e_b).astype(BF16)
            dk_ref[rows, sl] = (dk + dke * e_t).astype(BF16)
            db = db + dqe * qe - dke * ke
            dtot = _colsum(dstb.astype(F32) * stb.astype(F32)) * e_tot + _colsum(dke * ke)
            dg_ref[rows, sl] = _split_dot(tri_t, db) + dtot
            yield

        _run_staged([unit(hh, jj) for hh in range(hpb) for jj in range(SCAN_CB)])
        for hh in range(hpb):
            ds_scr[hh] = dstate[hh]

    ntb = t // SCAN_TB
    blk = lambda i: _block_order(ntb - 1 - i, ntb, rev)
    col = pl.BlockSpec((SCAN_TB, hpb * DH), lambda h, i: (blk(i), h))
    out = jax.ShapeDtypeStruct((t, D), F32)
    outb = jax.ShapeDtypeStruct((t, D), BF16)
    return pl.pallas_call(
        body, name="scan_bwd_" + ("bw" if rev else "fw"), grid=(NH // hpb, ntb),
        in_specs=[col] * 5 + [pl.BlockSpec((hpb, SCAN_CB, DH, DH), lambda h, i: (h, blk(i), 0, 0))],
        out_specs=[col] * 4,
        out_shape=[outb] * 3 + [out],
        scratch_shapes=[pltpu.VMEM((hpb, DH, DH), F32), pltpu.VMEM((hpb * SCAN_CB, CHUNK, DH), F32)],
        compiler_params=_cp("parallel", "arbitrary"),
    )(q, k, v, g, do, states)


W_IN_GRAD_CHUNKS = (("a", (0, 512)), ("b", (0, 256)), ("b", (256, 512)))
W_IN_REF = 6688


def _layout_w_in(w):
    return jnp.pad(w, ((0, 0), (0, W_IN_COLS - W_IN_REF)))


def _unlayout_w_in(d):
    return d[:, :W_IN_REF]


W_IN_PAD = 896


def _assemble_w_in(g):
    n, r, wp = g.shape
    tr = 256
    tiles = wp // DH

    def body(g_ref, o_ref):
        lane = lax.broadcasted_iota(jnp.int32, (tr, DH), 1)
        for t in range(W_IN_COLS // DH):
            acc = None
            for j in range(n):
                c = DH * t - W_IN_SHARD * j
                if c <= -DH or c >= W_IN_SHARD:
                    continue
                k, s = divmod(c, DH)
                lo = g_ref[j, :, k * DH:(k + 1) * DH] if 0 <= k < tiles else None
                hi = g_ref[j, :, (k + 1) * DH:(k + 2) * DH] if s and 0 <= k + 1 < tiles else None
                if s:
                    zero = jnp.zeros((tr, DH), g.dtype)
                    lo = zero if lo is None else pltpu.roll(lo, DH - s, 1)
                    hi = zero if hi is None else pltpu.roll(hi, DH - s, 1)
                    part = jnp.where(lane < DH - s, lo, hi)
                else:
                    part = lo
                acc = part if acc is None else acc + part
            o_ref[:, t * DH:(t + 1) * DH] = jnp.zeros((tr, DH), g.dtype) if acc is None else acc

    return pl.pallas_call(
        body, name="assemble_w_in", grid=(r // tr,),
        in_specs=[pl.BlockSpec((n, tr, wp), lambda i: (0, i, 0))],
        out_specs=pl.BlockSpec((tr, W_IN_COLS), lambda i: (i, 0)),
        out_shape=jax.ShapeDtypeStruct((r, W_IN_COLS), g.dtype),
        compiler_params=_cp("parallel"),
    )(g)


def _gate_cols(w):
    return jnp.pad(w, ((0, 0), (GOFF, GW - GOFF - D)))


def _gate_rows(w):
    return jnp.pad(w, ((GOFF, GW - GOFF - D), (0, 0)))


def _layout_wgk(w):
    r = w.shape[1]
    top = jnp.concatenate([w[0], jnp.zeros_like(w[0])], axis=1)
    bot = jnp.concatenate([jnp.zeros_like(w[1]), w[1]], axis=1)
    return jnp.concatenate([top, bot, jnp.zeros((DH - 2 * r, D), w.dtype)], axis=0)


def _unlayout_wgk(d, r=16):
    return jnp.stack([d[:r, :HW], d[r:2 * r, HW:]])


def _local_step(z, target, modc, modx, norms, onw, hg_lb, wgk, bgk, get_w_in, get_mix, get_ffn, send):
    n_pre1, n_post1, n_pre2, n_post2 = norms
    t = z[0].shape[0] + z[1].shape[0]
    tm = 1152 if t % 1152 == 0 else 256
    h1 = _prenorm(z, n_pre1, modc, modx, 0, 1, "prenorm1")
    w_in = get_w_in(h1)
    p = _matmul(h1, w_in, NN, BF16, "mm_in", t, 1024, D)
    q, v, k_f, k_b, g_f, g_b = _gates_fwd(p, hg_lb, wgk, bgk)
    o_f, st_f = _scan_fwd(q, k_f, v, g_f, False)
    o_b, st_b = _scan_fwd(q, k_b, v, g_b, True)
    y = _post_fwd(o_f, o_b, p, onw)
    w_br_hg, w_br_gla, w_out = get_mix(y)
    u1 = _matmul(y, w_br_hg, NN, BF16, "mm_br_hg", tm, GW, HW, a_off=0)
    u2 = _matmul(y, w_br_gla, NN, BF16, "mm_br_gla", tm, GW, HW, a_off=1)
    merged = _merge_fwd(p, u1, u2)
    y1 = _matmul(merged, w_out, NN, BF16, "mm_out", tm, 512, GW)
    z1, h2 = _mid_fwd(z, y1, n_post1, n_pre2, modc, modx)
    w_gu_t, w_down = get_ffn(h2)
    u, v_ff, act = _mm_gu_act(h2, w_gu_t[0], w_gu_t[1], "mm_gu", tm)
    y2 = _matmul(act, w_down, NN, BF16, "mm_down", t, 512, D_FF)
    dz, dy2, loss_vec, sm_final = _final(z1, y2, target, n_post2, modc, modx)
    du, dv_ff = _mm_down_dx_act(dy2, w_down, u, v_ff, "mm_down_dx", tm)
    d_w_down = _matmul(act, dy2, TN, BF16, "mm_down_dw", D_FF // 2, 1024, t)
    dh2 = _matmul((du, dv_ff), w_gu_t, NN, BF16, "mm_gu_dx", tm, 512, D_FF)
    d_w_gate_t = _matmul(du, h2, TN, BF16, "mm_gate_dw", D_FF // 2, 1024, t)
    d_w_up_t = _matmul(dv_ff, h2, TN, BF16, "mm_up_dw", D_FF // 2, 1024, t)
    dh2 = send(("w_down", "w_gate_t", "w_up_t"), (d_w_down, d_w_gate_t, d_w_up_t), dh2)
    dz, dy1, sm_mid = _mid_bwd(dh2, dz, z1, y1, n_post1, n_pre2, modc, modx)
    dmerged = _matmul(dy1, w_out, NT, BF16, "mm_out_dx", tm, GW, D)
    d_w_out = _matmul(merged, dy1, TN, BF16, "mm_out_dw", GW, 512, t)
    du1, du2, dgm = _merge_bwd(dmerged, p, u1, u2)
    dy_hg = _matmul(du1, w_br_hg, NT, BF16, "mm_br_hg_dx", tm, HW, GW)
    dy_gla = _matmul(du2, w_br_gla, NT, BF16, "mm_br_gla_dx", tm, HW, GW)
    d_w_br_hg = _matmul(y, du1, TN, BF16, "mm_br_hg_dw", HW, GW, t, a_off=0, m_out=HW)
    d_w_br_gla = _matmul(y, du2, TN, BF16, "mm_br_gla_dw", HW, GW, t, a_off=1, m_out=HW)
    dy_hg = send(("w_out", "w_br_hg", "w_br_gla"), (d_w_out, d_w_br_hg, d_w_br_gla), dy_hg)
    do, dgo, sm_post = _post_bwd(dy_hg, dy_gla, o_f, o_b, p, onw)
    dq_f, dk_f, dv_f, dg_f = _scan_bwd(q, k_f, v, g_f, do, st_f, False)
    dq_b, dk_b, dv_b, dg_b = _scan_bwd(q, k_b, v, g_b, do, st_b, True)
    dp, d_lb, d_wgk, d_bgk = _gates_bwd(p, hg_lb, wgk, bgk, dgm, dgo, dq_f, dq_b, dv_f, dv_b, dk_f, dk_b, dg_f, dg_b)
    d_w_in_a = _matmul(h1, dp, TN, BF16, "mm_in_dw_a", 512, 1024, t, a_off=0, m_out=D // 2)
    dp = send(("w_in_a",), (d_w_in_a,), dp)
    d_w_in_b = _matmul(h1, dp, TN, BF16, "mm_in_dw_b", 512, 1024, t, a_off=1, m_out=D // 2)
    dp = send(("w_in_b",), (d_w_in_b,), dp)
    dh1 = _matmul(dp, w_in, NT, BF16, "mm_in_dx", tm, 512, W_IN_COLS // 2)
    grad_x, sm_pre = _pre_bwd(dh1, dz, z, n_pre1, modc, modx)
    return dict(loss_vec=loss_vec, grad_x=grad_x, sm_final=sm_final, sm_mid=sm_mid, sm_post=sm_post, sm_pre=sm_pre,
                d_lb=d_lb, d_wgk=d_wgk, d_bgk=d_bgk)


MESH = pl.DeviceIdType.MESH
ANY = pl.BlockSpec(memory_space=pl.ANY)
N_REL = N_DEV - 1


def _place():
    return lax.axis_index("x"), lax.axis_index("y"), lax.axis_index("c")


def _slot(p):
    return 4 * p[0] + 2 * p[1] + p[2]


def _all_gather(arrays, name):
    n = len(arrays)

    def body(*refs):
        ins, outs = refs[:n], refs[n:2 * n]
        send_sems, recv_sems, local_sems = refs[2 * n:]
        x, y, c = _place()
        me, sibling = (x, y, c), (x, y, 1 - c)
        chips = [(1 - x, y), (x, 1 - y), (1 - x, 1 - y)]

        def copy(a, k, block, to, src=None):
            dst = outs[a].at[_slot(block)]
            return pltpu.make_async_remote_copy(
                src_ref=dst if src is None else src, dst_ref=dst,
                send_sem=send_sems.at[N_REL * a + k], recv_sem=recv_sems.at[N_REL * a + k],
                device_id=to, device_id_type=MESH)

        mine = [pltpu.make_async_copy(ins[a], outs[a].at[_slot(me)], local_sems.at[a]) for a in range(n)]
        for cp in mine:
            cp.start()
        first = []
        for a in range(n):
            first.append(copy(a, 0, me, sibling, src=ins[a]))
            first += [copy(a, 1 + j, me, (*chip, c), src=ins[a]) for j, chip in enumerate(chips)]
        for cp in first:
            cp.start()
        passed = []
        for j, chip in enumerate(chips):
            for a in range(n):
                copy(a, 1 + j, (*chip, c), me).wait_recv()
                fwd = copy(a, 4 + j, (*chip, c), sibling)
                fwd.start()
                passed.append(fwd)
        for a in range(n):
            copy(a, 0, sibling, me).wait_recv()
        for j, chip in enumerate(chips):
            for a in range(n):
                copy(a, 4 + j, (*chip, 1 - c), me).wait_recv()
        for cp in first + passed:
            cp.wait_send()
        for cp in mine:
            cp.wait()

    return pl.pallas_call(
        body, name=name,
        in_specs=[ANY] * n, out_specs=[ANY] * n,
        out_shape=[jax.ShapeDtypeStruct((N_DEV,) + a.shape, a.dtype) for a in arrays],
        scratch_shapes=[pltpu.SemaphoreType.DMA((N_REL * n,)), pltpu.SemaphoreType.DMA((N_REL * n,)),
                        pltpu.SemaphoreType.DMA((n,))],
    )(*arrays)


def _exchange(arrays, name):
    n = len(arrays)

    def body(*refs):
        ins, outs = refs[:n], refs[n:2 * n]
        send_sems, recv_sems, local_sems = refs[2 * n:]
        x, y, c = _place()
        me = _slot((x, y, c))
        mine = [pltpu.make_async_copy(ins[a].at[me], outs[a].at[me], local_sems.at[a]) for a in range(n)]
        for cp in mine:
            cp.start()
        copies = []
        for a in range(n):
            for k in range(1, N_DEV):
                flip = lambda v, bit: 1 - v if bit else v
                peer = (flip(x, k & 4), flip(y, k & 2), flip(c, k & 1))
                copies.append(pltpu.make_async_remote_copy(
                    src_ref=ins[a].at[_slot(peer)], dst_ref=outs[a].at[me],
                    send_sem=send_sems.at[N_REL * a + k - 1], recv_sem=recv_sems.at[N_REL * a + k - 1],
                    device_id=peer, device_id_type=MESH))
                copies[-1].start()
        i = 0
        for a in range(n):
            for k in range(1, N_DEV):
                flip = lambda v, bit: 1 - v if bit else v
                peer = (flip(x, k & 4), flip(y, k & 2), flip(c, k & 1))
                pltpu.make_async_remote_copy(
                    src_ref=ins[a].at[_slot(peer)], dst_ref=outs[a].at[_slot(peer)],
                    send_sem=send_sems.at[N_REL * a + k - 1], recv_sem=recv_sems.at[N_REL * a + k - 1],
                    device_id=peer, device_id_type=MESH).wait_recv()
                i += 1
        for cp in copies:
            cp.wait_send()
        for cp in mine:
            cp.wait()

    return pl.pallas_call(
        body, name=name,
        in_specs=[ANY] * n, out_specs=[ANY] * n,
        out_shape=[jax.ShapeDtypeStruct(a.shape, a.dtype) for a in arrays],
        scratch_shapes=[pltpu.SemaphoreType.DMA((N_REL * n,)), pltpu.SemaphoreType.DMA((N_REL * n,)),
                        pltpu.SemaphoreType.DMA((n,))],
    )(*arrays)


HBM = pl.BlockSpec(memory_space=pltpu.HBM)
SEM = pl.BlockSpec(memory_space=pltpu.SEMAPHORE)
EFFECT = pltpu.SideEffectType.DATAFLOW_SIDE_EFFECTING


def _peer_of(x, y, c, k):
    flip = lambda v, bit: 1 - v if bit else v
    return flip(x, k & 4), flip(y, k & 2), flip(c, k & 1)


def _view_whole(src, slot):
    return src


def _view_near(src, slot):
    return src


_view_near.peers = (1, 2, 4, 6)


def _view_block(src, slot):
    return src.at[slot]


W_IN_SHARD = W_IN_REF // N_DEV


def _view_window(rows):
    def view(src, slot):
        col0 = pl.multiple_of((W_IN_SHARD * slot // DH) * DH, DH)
        return src.at[pl.ds(rows[0], rows[1] - rows[0]), pl.ds(col0, D)]
    return view


def _split_copies(view, srcs, lands, send_sems, recv_sems, local_sems):
    x, y, c = _place()
    me = _slot((x, y, c))
    local, sends, waits = [], [], []
    for a, (src, land) in enumerate(zip(srcs, lands)):
        local.append(pltpu.make_async_copy(view(src, me), land.at[me], local_sems.at[a]))
        for k in getattr(view, "peers", range(1, N_DEV)):
            peer = _peer_of(x, y, c, k)
            mine = view(src, _slot(peer))
            sems = dict(send_sem=send_sems.at[N_REL * a + k - 1], recv_sem=recv_sems.at[N_REL * a + k - 1],
                        device_id=peer, device_id_type=MESH)
            sends.append(pltpu.make_async_remote_copy(src_ref=mine, dst_ref=land.at[me], **sems))
            waits.append(pltpu.make_async_remote_copy(src_ref=mine, dst_ref=land.at[_slot(peer)], **sems))
    return local, sends, waits


def _split_start(view, land_shapes, srcs, name, after):
    n = len(srcs)
    lands = [lax.empty(shp, s.dtype) for shp, s in zip(land_shapes, srcs)]

    def body(*refs):
        src_refs, land_refs = refs[:n], refs[n:2 * n]
        send_sems, recv_sems, local_sems = refs[2 * n + 1:2 * n + 4]
        token = refs[-1]
        local, sends, _ = _split_copies(view, src_refs, land_refs, send_sems, recv_sems, local_sems)
        for cp in local + sends:
            cp.start()
        token[...] = jnp.zeros_like(token)

    hbm = lambda a: pltpu.with_memory_space_constraint(a, pltpu.HBM)
    out = pl.pallas_call(
        body, name=name,
        out_shape=(pltpu.SemaphoreType.DMA((N_REL * n,)), pltpu.SemaphoreType.DMA((N_REL * n,)),
                   pltpu.SemaphoreType.DMA((n,)),
                   *[pltpu.HBM(s.shape, s.dtype) for s in srcs], *[pltpu.HBM(l.shape, l.dtype) for l in lands],
                   jax.ShapeDtypeStruct((8, DH), F32)),
        in_specs=[HBM] * (2 * n) + [ANY],
        out_specs=(SEM, SEM, SEM, *([HBM] * (2 * n)), pl.BlockSpec(memory_space=pltpu.VMEM)),
        input_output_aliases={i: 3 + i for i in range(2 * n)},
        compiler_params=pltpu.CompilerParams(has_side_effects=EFFECT),
    )(*[hbm(s) for s in srcs], *[hbm(l) for l in lands], after)
    handle = dict(view=view, n=n, sems=out[:3], srcs=list(out[3:3 + n]), lands=list(out[3 + n:3 + 2 * n]))
    return handle, out[-1]


def _split_wait(handle, name, after, srcs=None):
    view, n, sems, lands = handle["view"], handle["n"], handle["sems"], handle["lands"]
    srcs = handle["srcs"] if srcs is None else srcs
    afters = list(after) if isinstance(after, (list, tuple)) else [after]

    def body(*refs):
        src_refs, land_refs = refs[:n], refs[n:2 * n]
        send_sems, recv_sems, local_sems = refs[2 * n:2 * n + 3]
        local, _, waits = _split_copies(view, src_refs, land_refs, send_sems, recv_sems, local_sems)
        for cp in waits:
            cp.wait_send()
            cp.wait_recv()
        for cp in local:
            cp.wait()

    out = pl.pallas_call(
        body, name=name,
        out_shape=(*[pltpu.HBM(s.shape, s.dtype) for s in srcs], *[pltpu.HBM(l.shape, l.dtype) for l in lands]),
        in_specs=[HBM] * (2 * n) + [SEM, SEM, SEM] + [ANY] * len(afters),
        out_specs=tuple([HBM] * (2 * n)),
        input_output_aliases={i: i for i in range(2 * n)},
        compiler_params=pltpu.CompilerParams(has_side_effects=EFFECT),
    )(*srcs, *lands, *sems, *afters)
    handle["srcs"] = list(out[:n])
    return list(out[n:])


def _tie(x, token, name):
    def body(x_ref, t_ref, o_ref):
        pass

    return pl.pallas_call(
        body, name=name, out_shape=jax.ShapeDtypeStruct(x.shape, x.dtype),
        in_specs=[ANY, ANY], out_specs=ANY, input_output_aliases={0: 0},
    )(x, token)


def _forward_to_sibling(land, name):
    def body(land_ref, out_ref, send_sems, recv_sems):
        x, y, c = _place()
        sibling = (x, y, 1 - c)
        chips = [(1 - x, y), (x, 1 - y), (1 - x, 1 - y)]

        def copy(j, core):
            blk = _slot((*chips[j], core))
            return pltpu.make_async_remote_copy(src_ref=land_ref.at[blk], dst_ref=out_ref.at[blk],
                                                send_sem=send_sems.at[j], recv_sem=recv_sems.at[j],
                                                device_id=sibling, device_id_type=MESH)

        sends = [copy(j, c) for j in range(3)]
        for cp in sends:
            cp.start()
        for j in range(3):
            copy(j, 1 - c).wait_recv()
        for cp in sends:
            cp.wait_send()

    return pl.pallas_call(
        body, name=name, in_specs=[ANY], out_specs=ANY, input_output_aliases={0: 0},
        out_shape=jax.ShapeDtypeStruct(land.shape, land.dtype),
        scratch_shapes=[pltpu.SemaphoreType.DMA((3,)), pltpu.SemaphoreType.DMA((3,))],
    )(land)


def _mod_fwd(a, w, b):
    def body(a_ref, w_ref, b_ref, o_ref):
        o_ref[...] = _dot(_silu(a_ref[...]), w_ref[...], NN, precision=HI) + b_ref[...]

    return pl.pallas_call(
        body, name="mod_fwd", out_shape=jax.ShapeDtypeStruct((a.shape[0], w.shape[1]), F32),
        compiler_params=pltpu.CompilerParams(vmem_limit_bytes=VMEM_LIMIT),
    )(a, w, b)


def _mod_bwd(a, d, w):
    def body(a_ref, d_ref, w_ref, dw_ref, dc_ref):
        av = a_ref[...]
        dv = d_ref[...]
        dw_ref[...] = _dot(_silu(av), dv, TN, precision=HI)
        da = _dot(dv[0:8, :], w_ref[...], NT, precision=HI) * _dsilu(av[0:8, :])
        row = lax.broadcasted_iota(jnp.int32, da.shape, 0)
        dc_ref[...] = jnp.where(row == 0, da, 0.0)

    return pl.pallas_call(
        body, name="mod_bwd",
        out_shape=[jax.ShapeDtypeStruct(w.shape, F32), jax.ShapeDtypeStruct((8, w.shape[0]), F32)],
        compiler_params=pltpu.CompilerParams(vmem_limit_bytes=VMEM_LIMIT),
    )(a, d, w)


def _sum_devices(g):
    def body(g_ref, o_ref):
        acc = g_ref[0]
        for i in range(1, g.shape[0]):
            acc = acc + g_ref[i]
        o_ref[...] = acc

    return pl.pallas_call(body, name="sum_devices_%d" % g.shape[1],
                          out_shape=jax.ShapeDtypeStruct(g.shape[1:], F32))(g)


def _sum_windows(g, name):
    n, r, c = g.shape
    tr = 128

    def body(g_ref, o_ref):
        x, y, cc = _place()
        lane0 = (W_IN_SHARD * _slot((x, y, cc))) % DH
        acc = g_ref[0].astype(F32)
        for i in range(1, n):
            acc = acc + g_ref[i].astype(F32)
        o_ref[...] = pltpu.roll(acc, (c - lane0) % c, 1).T

    return pl.pallas_call(
        body, name=name, grid=(r // tr,),
        in_specs=[pl.BlockSpec((n, tr, c), lambda i: (0, i, 0))],
        out_specs=pl.BlockSpec((c, tr), lambda i: (0, i)),
        out_shape=jax.ShapeDtypeStruct((c, r), F32),
        compiler_params=_cp("parallel"),
    )(g)


def _adam_rows(r, c, n):
    budget = 6 * 1024 * 1024
    best = None
    for tr in range(16, r + 1, 16):
        if r % tr == 0 and tr * c * (2 * n + 28) <= budget:
            best = tr
    return best if best is not None else r


def _adamw(g, w, m, v, name):
    n, r, c = g.shape
    tr = _adam_rows(r, c, n)
    bc1 = 1.0 - ADAM_B1 ** ADAM_STEP
    bc2 = 1.0 - ADAM_B2 ** ADAM_STEP

    def body(g_ref, w_ref, m_ref, v_ref, go_ref, d_ref, mo_ref, vo_ref):
        grad = g_ref[0].astype(F32)
        for i in range(1, n):
            grad = grad + g_ref[i].astype(F32)
        go_ref[...] = grad
        m_new = ADAM_B1 * m_ref[...] + (1.0 - ADAM_B1) * grad
        v_new = ADAM_B2 * v_ref[...] + (1.0 - ADAM_B2) * (grad * grad)
        mo_ref[...] = m_new
        vo_ref[...] = v_new
        d_ref[...] = -ADAM_LR * ((m_new / bc1) / (jnp.sqrt(v_new / bc2) + ADAM_EPS) + ADAM_WD * w_ref[...])

    blk = pl.BlockSpec((tr, c), lambda i: (i, 0))
    out = jax.ShapeDtypeStruct((r, c), F32)
    return pl.pallas_call(
        body, name=name, grid=(r // tr,),
        in_specs=[pl.BlockSpec((n, tr, c), lambda i: (0, i, 0)), blk, blk, blk],
        out_specs=[blk] * 4, out_shape=[out] * 4,
        compiler_params=_cp("parallel"),
    )(g, w, m, v)


ADAM_ROWS3 = 168


def _adam_math(grad, w, m, v):
    bc1 = 1.0 - ADAM_B1 ** ADAM_STEP
    bc2 = 1.0 - ADAM_B2 ** ADAM_STEP
    m_new = ADAM_B1 * m + (1.0 - ADAM_B1) * grad
    v_new = ADAM_B2 * v + (1.0 - ADAM_B2) * (grad * grad)
    delta = -ADAM_LR * ((m_new / bc1) / (jnp.sqrt(v_new / bc2) + ADAM_EPS) + ADAM_WD * w)
    return delta, m_new, v_new


def _adamw_rows3(g, w3, m3, v3, name):
    r, _, c = w3.shape
    n = ADAM_ROWS3
    starts = list(range(0, r - n, n)) + [r - n]

    def body(g_hbm, w_hbm, m_hbm, v_hbm, go_hbm, d_hbm, mo_hbm, vo_hbm, gbuf, ibuf, obuf, in_sems, out_sems):
        def fetch(p):
            r0, slot = starts[p], p % 2
            g0 = (r0 // 8) * 8
            cps = [pltpu.make_async_copy(g_hbm.at[pl.ds(g0, n + 8)], gbuf.at[slot], in_sems.at[slot, 0])]
            cps += [pltpu.make_async_copy(h.at[pl.ds(r0, n), 0], ibuf.at[slot, k], in_sems.at[slot, 1 + k])
                    for k, h in enumerate((w_hbm, m_hbm, v_hbm))]
            for cp in cps:
                cp.start()
            return cps

        pending, outs = fetch(0), []
        for p, r0 in enumerate(starts):
            slot = p % 2
            nxt = fetch(p + 1) if p + 1 < len(starts) else []
            for cp in pending:
                cp.wait()
            grad = gbuf[slot, pl.ds(r0 - (r0 // 8) * 8, n), :]
            delta, m_new, v_new = _adam_math(grad, ibuf[slot, 0], ibuf[slot, 1], ibuf[slot, 2])
            for cp in outs:
                cp.wait()
            for k, val in enumerate((grad, delta, m_new, v_new)):
                obuf[slot, k] = val
            outs = [pltpu.make_async_copy(obuf.at[slot, k], h.at[pl.ds(r0, n), 0], out_sems.at[slot, k])
                    for k, h in enumerate((go_hbm, d_hbm, mo_hbm, vo_hbm))]
            for cp in outs:
                cp.start()
            pending = nxt
        for cp in outs:
            cp.wait()

    out = jax.ShapeDtypeStruct(w3.shape, F32)
    return pl.pallas_call(
        body, name=name, in_specs=[ANY] * 4, out_specs=[ANY] * 4, out_shape=[out] * 4,
        scratch_shapes=[pltpu.VMEM((2, n + 8, c), F32), pltpu.VMEM((2, 3, n, c), F32), pltpu.VMEM((2, 4, n, c), F32),
                        pltpu.SemaphoreType.DMA((2, 4)), pltpu.SemaphoreType.DMA((2, 4))],
        compiler_params=pltpu.CompilerParams(vmem_limit_bytes=VMEM_LIMIT),
    )(g, w3, m3, v3)


def kernel(x, c, ctx, c_ctx, w_mod, b_mod, norm_pre1, norm_post1, norm_pre2, norm_post2, w_in, hg_lb, hg_onorm, gla_w_gk, gla_b_gk, gla_onorm, w_br_hg, w_br_gla, w_out, w_ff_gate, w_ff_up, w_ff_down, loss_target, m_c_ctx, m_w_mod, m_b_mod, m_norm_pre1, m_norm_post1, m_norm_pre2, m_norm_post2, m_w_in, m_hg_lb, m_hg_onorm, m_gla_w_gk, m_gla_b_gk, m_gla_onorm, m_w_br_hg, m_w_br_gla, m_w_out, m_w_ff_gate, m_w_ff_up, m_w_ff_down, v_c_ctx, v_w_mod, v_b_mod, v_norm_pre1, v_norm_post1, v_norm_pre2, v_norm_post2, v_w_in, v_hg_lb, v_hg_onorm, v_gla_w_gk, v_gla_b_gk, v_gla_onorm, v_w_br_hg, v_w_br_gla, v_w_out, v_w_ff_gate, v_w_ff_up, v_w_ff_down):
    xi, yi, ci = lax.axis_index("x"), lax.axis_index("y"), lax.axis_index("c")
    me = 4 * xi + 2 * yi + ci
    t = CTX + x.shape[1]

    c_all, lb_g, wgk_g, bgk_g = _all_gather([c, hg_lb, gla_w_gk[0], gla_b_gk[0]], "ag_small")
    tr_ = lambda a: jnp.swapaxes(a[0], 0, 1)
    big = [w_in[0], w_br_hg[0], w_br_gla[0], w_out[0], tr_(w_ff_gate), tr_(w_ff_up), w_ff_down[0]]
    big_bf = [w.astype(BF16) for w in big]
    big_bf[0] = jnp.pad(big_bf[0], ((0, 0), (0, W_IN_PAD - W_IN_SHARD)))
    cols = lambda g: jnp.transpose(g, (1, 0, 2)).reshape(g.shape[1], N_DEV * g.shape[2])

    def get_w_in(after):
        land, = _split_wait(w_in_handle, "ag_w_in_wait", after)
        return _assemble_w_in(_forward_to_sibling(land, "ag_w_in_forward"))

    def get_mix(after):
        g_brh, g_brg, g_out = _split_wait(mix_handle, "ag_mix_wait", after)
        return _gate_cols(cols(g_brh)), _gate_cols(cols(g_brg)), _gate_rows(g_out.reshape(D, D))

    def get_ffn(after):
        g_gate, g_up, g_down = _split_wait(ffn_handle, "ag_ffn_wait", after)
        return (g_gate.reshape(D_FF, D), g_up.reshape(D_FF, D)), g_down.reshape(D_FF, D)

    hg_lb_full = jnp.transpose(lb_g, (1, 2, 0, 3)).reshape(2, 2, HW)
    wgk_k = _layout_wgk(jnp.transpose(wgk_g, (1, 2, 0, 3)).reshape(2, 16, HW)).astype(BF16)
    bgk_k = jnp.transpose(bgk_g, (1, 0, 2)).reshape(1, D)
    onw = jnp.concatenate([jnp.tile(hg_onorm, (1, NH // 2)), jnp.tile(gla_onorm, (1, NH // 2))], axis=1)

    n_mod = w_mod.shape[2]
    a9 = jnp.concatenate([c_ctx[None], c_all[:, 0], jnp.zeros((16 - 1 - N_DEV, D), F32)], axis=0)
    b_loc = lax.dynamic_slice(b_mod, (0, me * n_mod), (1, n_mod))
    s_loc = _mod_fwd(a9, w_mod[0], b_loc)
    s_all, = _all_gather([s_loc], "ag_mod")
    mod_all = jnp.transpose(s_all, (1, 0, 2)).reshape(16, N_DEV * n_mod)
    pad8 = lambda m: jnp.concatenate([m.reshape(6, D), jnp.zeros((2, D), F32)], axis=0)
    modc = pad8(mod_all[0])
    modx = pad8(lax.dynamic_slice(mod_all, (1 + me, 0), (1, N_DEV * n_mod))[0])

    gathered = lambda arrs: [(N_DEV,) + a.shape for a in arrs]
    w_in_handle, tok = _split_start(_view_near, gathered(big_bf[:1]), big_bf[:1], "ag_w_in_start", s_all)
    mix_handle, tok = _split_start(_view_whole, gathered(big_bf[1:4]), big_bf[1:4], "ag_mix_start", tok)
    ffn_handle, tok = _split_start(_view_whole, gathered(big_bf[4:]), big_bf[4:], "ag_ffn_start", tok)

    z = (ctx[0], x[0])
    modx = _tie(modx, tok, "tie_mod")
    norms = (norm_pre1, norm_post1, norm_pre2, norm_post2)
    shard = lambda d: jnp.transpose(d.reshape(d.shape[0], N_DEV, -1), (1, 0, 2)).astype(BF16)
    rowshard = lambda d: d.reshape(N_DEV, d.shape[0] // N_DEV, d.shape[1]).astype(BF16)
    sent, w_in_grad = [], {}

    def send_w_in(i, x_after):
        half, rows = W_IN_GRAD_CHUNKS[i]
        handle, tok = _split_start(_view_window(rows), [(N_DEV, rows[1] - rows[0], D)], w_in_grad[half],
                                   "grads_w_in%d_start" % i, x_after)
        w_in_grad[half] = handle["srcs"]
        sent.append(("w_in%d" % i, ["w_in#%d" % i], handle))
        return tok

    def send(names, grads, x_after):
        if names == ("w_in_a",):
            w_in_grad["a"] = list(grads)
            return _tie(x_after, send_w_in(0, x_after), "tie_w_in0")
        if names == ("w_in_b",):
            w_in_grad["b"] = list(grads)
            return x_after
        arrs, leaves = [], []
        for nm, g in zip(names, grads):
            if nm in ("w_gate_t", "w_up_t"):
                arrs.append(rowshard(g))
                leaves.append({"w_gate_t": "w_ff_gate", "w_up_t": "w_ff_up"}[nm])
            elif nm == "w_down":
                arrs.append(rowshard(g))
                leaves.append("w_ff_down")
            elif nm == "w_out":
                arrs.append(rowshard(g[GOFF:GOFF + D]))
                leaves.append(nm)
            else:
                arrs.append(shard(g[:, GOFF:GOFF + D]))
                leaves.append(nm)
        handle, tok = _split_start(_view_block, [a.shape for a in arrs], arrs, "grads_%s_start" % names[0], x_after)
        sent.append((names[0], leaves, handle))
        return _tie(x_after, tok, "tie_" + names[0])

    r = _local_step(z, loss_target[0], modc, modx, norms, onw, hg_lb_full, wgk_k, bgk_k,
                    get_w_in, get_mix, get_ffn, send)
    grad_x = r["grad_x"][None]

    sm_pre, sm_mid, sm_fin = r["sm_pre"], r["sm_mid"], r["sm_final"]
    dmodc = jnp.stack([sm_pre[0], sm_pre[2], sm_mid[4], sm_mid[0], sm_mid[2], sm_fin[0]]).reshape(-1)
    dmodx = jnp.stack([sm_pre[1], sm_pre[3], sm_mid[5], sm_mid[1], sm_mid[3], sm_fin[1]]).reshape(-1)
    on = r["sm_post"][0].reshape(NH, DH)
    pieces = [dmodc, dmodx, sm_pre[4], sm_mid[7], sm_mid[6], sm_fin[2], on[:NH // 2].sum(0), on[NH // 2:].sum(0),
              r["d_lb"][:2].reshape(-1), _unlayout_wgk(r["d_wgk"]).reshape(-1), r["d_bgk"][0]]
    loss_local = (0.5 / D) * jnp.sum(r["loss_vec"])
    pieces.append(jnp.concatenate([loss_local.reshape(1), jnp.zeros((DH - 1,), F32)]))
    sizes = [p.shape[0] for p in pieces]
    pack = jnp.concatenate(pieces).reshape(-1, DH)
    moms = [(m_w_in, v_w_in), (m_w_br_hg, v_w_br_hg), (m_w_br_gla, v_w_br_gla), (m_w_out, v_w_out),
            (m_w_ff_gate, v_w_ff_gate), (m_w_ff_up, v_w_ff_up), (m_w_ff_down, v_w_ff_down)]
    names = ["w_in", "w_br_hg", "w_br_gla", "w_out", "w_ff_gate", "w_ff_up", "w_ff_down"]
    wmv = {nm: (w, m, v) for nm, w, (m, v) in zip(names, big, moms)}
    res = {}

    def update(nm):
        w, m, v = wmv[nm]
        if nm in ("w_ff_gate", "w_ff_up"):
            outs = _adamw(recv[nm], w, tr_(m), tr_(v), "adamw_" + nm)
            res[nm] = [jnp.swapaxes(o, 0, 1)[None] for o in outs]
        else:
            res[nm] = [o[None] for o in _adamw(recv[nm], w, m[0], v[0], "adamw_" + nm)]

    small_handle, tok = _split_start(_view_whole, [(N_DEV,) + pack.shape], [pack], "small_grads_start", pack)
    tok = send_w_in(1, tok)
    recv = {}
    for first, leaves, handle in sent:
        if not first.startswith("w_in"):
            recv.update(zip(leaves, _split_wait(handle, "grads_%s_wait" % first, tok)))
    update("w_ff_gate")
    update("w_ff_up")
    pack_all, = _split_wait(small_handle, "small_grads_wait", [res["w_ff_gate"][0], res["w_ff_up"][0]])
    tot = _sum_devices(pack_all).reshape(-1)
    offs = [sum(sizes[:i]) for i in range(len(sizes))]
    part = lambda i: tot[offs[i]:offs[i] + sizes[i]]
    dmodc_t, dmodx_t = part(0), part(1)
    g_b_mod = (dmodc_t + dmodx_t)[None]
    g_norms = [part(i)[None] for i in (2, 3, 4, 5)]
    g_hg_on, g_gla_on = part(6)[None], part(7)[None]
    lb0 = lax.dynamic_slice(part(8).reshape(2, HW), (0, me * (HW // N_DEV)), (2, HW // N_DEV))
    g_hg_lb = jnp.stack([lb0, -lb0])
    g_wgk = lax.dynamic_slice(part(9).reshape(2, 16, HW), (0, 0, me * (HW // N_DEV)), (2, 16, HW // N_DEV))[None]
    g_bgk = lax.dynamic_slice(part(10).reshape(2, HW), (0, me * (HW // N_DEV)), (2, HW // N_DEV))[None]
    loss = part(11)[0]

    dmx_all = pack_all.reshape(N_DEV, -1)[:, sizes[0]:sizes[0] + sizes[1]]
    d9 = jnp.concatenate([lax.dynamic_slice(dmodc_t[None], (0, me * n_mod), (1, n_mod)),
                          lax.dynamic_slice(dmx_all, (0, me * n_mod), (N_DEV, n_mod)),
                          jnp.zeros((16 - 1 - N_DEV, n_mod), F32)], axis=0)
    g_w_mod, dcc_part = _mod_bwd(a9, d9, w_mod[0])
    cctx_handle, tok = _split_start(_view_whole, [(N_DEV,) + dcc_part.shape], [dcc_part], "c_ctx_start", dcc_part)
    tok = send_w_in(2, tok)
    recv["w_ff_down"] = _tie(recv["w_ff_down"], tok, "tie_down")
    update("w_ff_down")
    res["w_mod"] = [o[None] for o in _adamw(g_w_mod[None], w_mod[0], m_w_mod[0], v_w_mod[0], "adamw_w_mod")]
    for nm in ("w_out", "w_br_hg", "w_br_gla"):
        update(nm)
    dcc_all, = _split_wait(cctx_handle, "c_ctx_wait", [res["w_ff_down"][0], res["w_mod"][0]])
    g_c_ctx = _sum_devices(dcc_all)[0]

    small = [("c_ctx", c_ctx, m_c_ctx, v_c_ctx, g_c_ctx), ("b_mod", b_mod, m_b_mod, v_b_mod, g_b_mod),
             ("norm_pre1", norm_pre1, m_norm_pre1, v_norm_pre1, g_norms[0]),
             ("norm_post1", norm_post1, m_norm_post1, v_norm_post1, g_norms[1]),
             ("norm_pre2", norm_pre2, m_norm_pre2, v_norm_pre2, g_norms[2]),
             ("norm_post2", norm_post2, m_norm_post2, v_norm_post2, g_norms[3]),
             ("hg_lb", hg_lb, m_hg_lb, v_hg_lb, g_hg_lb), ("hg_onorm", hg_onorm, m_hg_onorm, v_hg_onorm, g_hg_on),
             ("gla_w_gk", gla_w_gk, m_gla_w_gk, v_gla_w_gk, g_wgk), ("gla_b_gk", gla_b_gk, m_gla_b_gk, v_gla_b_gk, g_bgk),
             ("gla_onorm", gla_onorm, m_gla_onorm, v_gla_onorm, g_gla_on)]
    flat = lambda k: jnp.concatenate([s[k].reshape(-1) for s in small]).reshape(-1, DH)
    outs = _adamw(flat(4)[None], flat(1), flat(2), flat(3), "adamw_small")
    off = 0
    for nm, w, _, _, _ in small:
        res[nm] = [o.reshape(-1)[off:off + w.size].reshape(w.shape) for o in outs]
        off += w.size

    done = [res[nm][0] for nm in names[1:]] + [res["w_mod"][0], outs[0]]
    sums = []
    for i, (first, leaves, handle) in enumerate(s for s in sent if s[0].startswith("w_in")):
        half = W_IN_GRAD_CHUNKS[i][0]
        land, = _split_wait(handle, "grads_%s_wait" % first, done, srcs=w_in_grad[half])
        w_in_grad[half] = handle["srcs"]
        sums.append(_sum_windows(land, "sum_windows%d" % i))
    major = lambda a: jnp.transpose(a, (2, 0, 1))
    outs = _adamw_rows3(jnp.concatenate(sums, axis=1), major(w_in), major(m_w_in), major(v_w_in), "adamw_w_in")
    res["w_in"] = [jnp.transpose(o, (1, 2, 0)) for o in outs]

    order = ["c_ctx", "w_mod", "b_mod", "norm_pre1", "norm_post1", "norm_pre2", "norm_post2", "w_in", "hg_lb",
             "hg_onorm", "gla_w_gk", "gla_b_gk", "gla_onorm", "w_br_hg", "w_br_gla", "w_out", "w_ff_gate", "w_ff_up",
             "w_ff_down"]
    return (loss, grad_x, *[res[n][k] for k in range(4) for n in order])
```

```python
import functools

import jax
import jax.numpy as jnp
from jax import lax
from jax.experimental import pallas as pl
from jax.experimental.pallas import tpu as pltpu

F32 = jnp.float32
BF16 = jnp.bfloat16
HI = lax.Precision.HIGHEST

N_DEV = 8
D = 1024
CTX = 256
HW = 512
DH = 128
NH = 8
D_FF = 2816
EPS = 1e-6
GLA_NORM = 16.0
CHUNK = 64
TR = 256
NCT = CTX // TR
W_IN_COLS = 7168
MAIN0 = 0
LR0 = 4608
GW = 1152
GOFF = 32
GATE_HG0 = LR0
GATE_GLA0 = LR0 + D
LEVELS = (32, 16, 8)
EXP_CLAMP = 80.0
VMEM_LIMIT = 48 * 1024 * 1024

ADAM_LR, ADAM_B1, ADAM_B2, ADAM_EPS, ADAM_WD, ADAM_STEP = 0.001, 0.9, 0.999, 1e-08, 0.01, 10


def _cp(*sem):
    return pltpu.CompilerParams(dimension_semantics=sem, vmem_limit_bytes=VMEM_LIMIT)


def _sig(x):
    return jax.nn.sigmoid(x)


def _silu(x):
    return x * _sig(x)


def _dsilu(x):
    s = _sig(x)
    return s * (1.0 + x * (1.0 - s))


def _rstd(x):
    return lax.rsqrt(jnp.mean(x * x, axis=-1, keepdims=True) + EPS)


def _rms_bwd(a, y, r):
    return r * (a - y * (r * r) * jnp.mean(a * y, axis=-1, keepdims=True))


def _colsum(x):
    return jnp.sum(x, axis=0, keepdims=True)


def _dot(a, b, dims, precision=None):
    return lax.dot_general(a, b, (dims, ((), ())), preferred_element_type=F32, precision=precision)


NN = ((1,), (0,))
NT = ((1,), (1,))
TN = ((0,), (0,))

SCAN_HEADS_FWD = 4
SCAN_HEADS_BWD = 4


def _split_dot(m, x):
    mb = m.astype(BF16)
    x1 = x.astype(BF16)
    r1 = x - x1.astype(F32)
    x2 = r1.astype(BF16)
    x3 = (r1 - x2.astype(F32)).astype(BF16)
    return _dot(mb, x1, NN) + _dot(mb, x2, NN) + _dot(mb, x3, NN)


def _matmul(a, b, dims, out_dtype, name, tm, tn, tk, a_off=0, m_out=None):
    a_pair = isinstance(a, (tuple, list))
    as_ = list(a) if a_pair else [a]
    a = as_[0]
    pair = isinstance(b, (tuple, list))
    bs = list(b) if pair else [b]
    b1 = bs[0]
    rows = b1.shape[0] * len(bs)
    half = None
    if dims == NN:
        m, k, n = a.shape[0], rows, b1.shape[1]
        a_spec = pl.BlockSpec((tm, tk), lambda i, j, kk: (i, kk + a_off))
        half = b1.shape[0] // tk
        if a_pair:
            assert pair and a.shape[1] == b1.shape[0] and a_off == 0
            a_spec = [pl.BlockSpec((tm, tk), lambda i, j, kk: (i, jnp.minimum(kk, half - 1))),
                      pl.BlockSpec((tm, tk), lambda i, j, kk: (i, jnp.maximum(kk - half, 0)))]
        b_maps = [lambda i, j, kk: (kk, j)] if not pair else [
            lambda i, j, kk: (jnp.minimum(kk, half - 1), j), lambda i, j, kk: (jnp.maximum(kk - half, 0), j)]
        b_specs = [pl.BlockSpec((tk, tn), f) for f in b_maps]
        axis = 2
    elif dims == NT:
        m, k, n = a.shape[0], b1.shape[1], rows
        a_spec = pl.BlockSpec((tm, tk), lambda i, j, kk: (i, kk + a_off))
        half = b1.shape[0] // tn
        b_maps = [lambda i, j, kk: (j, kk)] if not pair else [
            lambda i, j, kk: (jnp.minimum(j, half - 1), kk), lambda i, j, kk: (jnp.maximum(j - half, 0), kk)]
        b_specs = [pl.BlockSpec((tn, tk), f) for f in b_maps]
        axis = 1
    else:
        assert not pair
        m, k = (a.shape[1] if m_out is None else m_out), a.shape[0]
        n = b1.shape[1]
        a_spec = pl.BlockSpec((tk, tm), lambda i, j, kk: (kk, i + a_off))
        b_specs = [pl.BlockSpec((tk, tn), lambda i, j, kk: (kk, j))]
    assert m % tm == 0 and n % tn == 0 and k % tk == 0, (name, m, n, k, tm, tn, tk)
    nk = k // tk
    nb = len(bs)
    na = len(as_)
    assert na == 1 or dims == NN

    def body(*refs):
        a_refs, refs = refs[:na], refs[na:]
        o_ref = refs[nb]
        if pair:
            bv = jnp.where(pl.program_id(axis) < half, refs[0][...], refs[1][...])
        else:
            bv = refs[0][...]
        av = a_refs[0][...] if na == 1 else jnp.where(pl.program_id(2) < half, a_refs[0][...], a_refs[1][...])
        part = _dot(av, bv, dims)
        if nk == 1:
            o_ref[...] = part.astype(o_ref.dtype)
            return
        acc_ref = refs[nb + 1]
        kk = pl.program_id(2)

        @pl.when(kk == 0)
        def _():
            acc_ref[...] = part

        @pl.when(kk > 0)
        def _():
            acc_ref[...] += part

        @pl.when(kk == nk - 1)
        def _():
            o_ref[...] = acc_ref[...].astype(o_ref.dtype)

    return pl.pallas_call(
        body,
        name=name,
        grid=(m // tm, n // tn, nk),
        in_specs=(a_spec if a_pair else [a_spec]) + b_specs,
        out_specs=pl.BlockSpec((tm, tn), lambda i, j, kk: (i, j)),
        out_shape=jax.ShapeDtypeStruct((m, n), out_dtype),
        scratch_shapes=[] if nk == 1 else [pltpu.VMEM((tm, tn), F32)],
        compiler_params=_cp("parallel", "parallel", "arbitrary"),
    )(*as_, *bs)


def _mm_gu_act(h, w_gate_t, w_up_t, name, tm):
    t = h.shape[0]
    tn = D_FF // 2

    def body(a_ref, bg_ref, bu_ref, u_ref, v_ref, act_ref):
        a = a_ref[...]
        u = _dot(a, bg_ref[...], NT)
        v = _dot(a, bu_ref[...], NT)
        u_ref[...] = u.astype(BF16)
        v_ref[...] = v.astype(BF16)
        act_ref[...] = (_silu(u) * v).astype(BF16)

    wspec = pl.BlockSpec((tn, D), lambda i, j: (j, 0))
    ospec = pl.BlockSpec((tm, tn), lambda i, j: (i, j))
    out = jax.ShapeDtypeStruct((t, D_FF), BF16)
    return pl.pallas_call(
        body, name=name, grid=(t // tm, D_FF // tn),
        in_specs=[pl.BlockSpec((tm, D), lambda i, j: (i, 0)), wspec, wspec],
        out_specs=[ospec] * 3, out_shape=[out] * 3,
        compiler_params=_cp("parallel", "parallel"),
    )(h, w_gate_t, w_up_t)


def _mm_down_dx_act(dy, w_down, u, v, name, tm):
    t = dy.shape[0]
    tn = D_FF // 2

    def body(a_ref, b_ref, u_ref, v_ref, du_ref, dv_ref):
        dact = _dot(a_ref[...], b_ref[...], NT)
        u = u_ref[...].astype(F32)
        du_ref[...] = (dact * v_ref[...].astype(F32) * _dsilu(u)).astype(BF16)
        dv_ref[...] = (dact * _silu(u)).astype(BF16)

    ospec = pl.BlockSpec((tm, tn), lambda i, j: (i, j))
    out = jax.ShapeDtypeStruct((t, D_FF), BF16)
    return pl.pallas_call(
        body, name=name, grid=(t // tm, D_FF // tn),
        in_specs=[pl.BlockSpec((tm, D), lambda i, j: (i, 0)), pl.BlockSpec((tn, D), lambda i, j: (j, 0)), ospec, ospec],
        out_specs=[ospec] * 2, out_shape=[out] * 2,
        compiler_params=_cp("parallel", "parallel"),
    )(dy, w_down, u, v)


def _row(c):
    return pl.BlockSpec((TR, c), lambda i: (i, 0))


def _rowcol(width, cb):
    return pl.BlockSpec((TR, width), lambda i: (i, cb))


def _full(shape):
    return pl.BlockSpec(shape, lambda i: (0,) * len(shape))


def _mod_row(mc_ref, mx_ref, k, is_ctx):
    return jnp.where(is_ctx, mc_ref[k:k + 1, :], mx_ref[k:k + 1, :])


def _z_specs():
    return [pl.BlockSpec((TR, D), lambda i: (jnp.minimum(i, NCT - 1), 0)),
            pl.BlockSpec((TR, D), lambda i: (jnp.maximum(i - NCT, 0), 0))]


def _z_tile(c_ref, x_ref, is_ctx):
    return jnp.where(is_ctx, c_ref[...], x_ref[...])


def _acc_row(ref, k, val):
    ref[k:k + 1, :] += val


def _acc_mod(ref, k, is_ctx, val):
    zero = jnp.zeros_like(val)
    ref[k:k + 1, :] += jnp.where(is_ctx, val, zero)
    ref[k + 1:k + 2, :] += jnp.where(is_ctx, zero, val)


def _prenorm(z, nw, modc, modx, i_shift, i_scale, name):
    t = z[0].shape[0] + z[1].shape[0]

    def body(zc_ref, zx_ref, nw_ref, mc_ref, mx_ref, h_ref):
        is_ctx = pl.program_id(0) < NCT
        x = _z_tile(zc_ref, zx_ref, is_ctx)
        n = x * _rstd(x) * nw_ref[...]
        h = n * (1.0 + _mod_row(mc_ref, mx_ref, i_scale, is_ctx)) + _mod_row(mc_ref, mx_ref, i_shift, is_ctx)
        h_ref[...] = h.astype(BF16)

    return pl.pallas_call(
        body, name=name, grid=(t // TR,),
        in_specs=_z_specs() + [_full((1, D)), _full((8, D)), _full((8, D))],
        out_specs=_row(D),
        out_shape=jax.ShapeDtypeStruct((t, D), BF16),
        compiler_params=_cp("parallel"),
    )(*z, nw, modc, modx)


def _hg_lb(lb_ref, d):
    a0 = lb_ref[0, d:d + 1, :]
    a1 = lb_ref[1, d:d + 1, :]
    mx = jnp.maximum(a0, a1)
    e0 = jnp.exp(a0 - mx)
    e1 = jnp.exp(a1 - mx)
    return e0 / (e0 + e1)


def _log_sigmoid(x):
    return jnp.minimum(x, 0.0) - jnp.log(1.0 + jnp.exp(-jnp.abs(x)))


def _gates_fwd(p, hg_lb, wgk, bgk):
    t = p.shape[0]
    seg = lambda j: _rowcol(HW, MAIN0 // HW + j)

    def body(hq_ref, hi_ref, hf_ref, hb_ref, gq_ref, gk_ref, gv_ref, lr_ref, lb_ref, wgk_ref, bgk_ref,
             q_ref, v_ref, kf_ref, kb_ref, gf_ref, gb_ref):
        q_ref[:, :HW] = _silu(hq_ref[...].astype(F32)).astype(BF16)
        q_ref[:, HW:] = (gq_ref[...].astype(F32) * (DH ** -0.5)).astype(BF16)
        v_ref[:, :HW] = hi_ref[...]
        v_ref[:, HW:] = gv_ref[...]
        xg = _dot(lr_ref[...].astype(BF16), wgk_ref[...], NN) + bgk_ref[...]
        for d, (raw_ref, k_ref, g_ref) in enumerate(((hf_ref, kf_ref, gf_ref), (hb_ref, kb_ref, gb_ref))):
            lbd = _hg_lb(lb_ref, d)
            f = lbd + (1.0 - lbd) * _sig(raw_ref[...].astype(F32))
            k_ref[:, :HW] = (1.0 - f).astype(BF16)
            k_ref[:, HW:] = gk_ref[...]
            g_ref[:, :HW] = jnp.log(f)
            g_ref[:, HW:] = _log_sigmoid(xg[:, d * HW:(d + 1) * HW]) * (1.0 / GLA_NORM)

    out = jax.ShapeDtypeStruct((t, D), F32)
    outb = jax.ShapeDtypeStruct((t, D), BF16)
    return pl.pallas_call(
        body, name="gates_fwd", grid=(t // TR,),
        in_specs=[seg(0), seg(1), seg(2), seg(3), seg(5), seg(6), seg(7), _rowcol(DH, LR0 // DH),
                  _full((2, 2, HW)), _full((DH, D)), _full((1, D))],
        out_specs=[_row(D)] * 6,
        out_shape=[outb] * 4 + [out] * 2,
        compiler_params=_cp("parallel"),
    )(p, p, p, p, p, p, p, p, hg_lb, wgk, bgk)


def _post_fwd(o_fw, o_bw, p, onw):
    t = o_fw.shape[0]

    def body(of_ref, ob_ref, g1_ref, g2_ref, w_ref, y_ref):
        for h in range(NH):
            sl = slice(h * DH, (h + 1) * DH)
            o = of_ref[:, sl] + ob_ref[:, sl]
            g_ref = g1_ref if h < NH // 2 else g2_ref
            gs = slice((h % (NH // 2)) * DH, (h % (NH // 2) + 1) * DH)
            n = o * _rstd(o) * w_ref[:, sl]
            y_ref[:, sl] = (n * _silu(g_ref[:, gs].astype(F32))).astype(BF16)

    return pl.pallas_call(
        body, name="post_fwd", grid=(t // TR,),
        in_specs=[_row(D), _row(D), _rowcol(HW, MAIN0 // HW + 4), _rowcol(HW, MAIN0 // HW + 8), _full((1, D))],
        out_specs=_row(D),
        out_shape=jax.ShapeDtypeStruct((t, D), BF16),
        compiler_params=_cp("parallel"),
    )(o_fw, o_bw, p, p, onw)


def _gate_window_specs(col0):
    return [_rowcol(HW, col0 // HW), _rowcol(HW, col0 // HW + 1), _rowcol(DH, (col0 + 2 * HW) // DH)]


def _gate_window(refs):
    return jnp.concatenate([r[...].astype(F32) for r in refs], axis=1)


def _merge_fwd(p, u1, u2):
    t = p.shape[0]

    def body(a0, a1, a2, b0, b1, b2, u1_ref, u2_ref, m_ref):
        f = lambda r: r[...].astype(F32)
        m_ref[...] = (_sig(_gate_window((a0, a1, a2))) * f(u1_ref)
                      + _sig(_gate_window((b0, b1, b2))) * f(u2_ref)).astype(BF16)

    return pl.pallas_call(
        body, name="merge_fwd", grid=(t // TR,),
        in_specs=_gate_window_specs(GATE_HG0) + _gate_window_specs(GATE_GLA0) + [_row(GW), _row(GW)],
        out_specs=_row(GW),
        out_shape=jax.ShapeDtypeStruct((t, GW), BF16),
        compiler_params=_cp("parallel"),
    )(p, p, p, p, p, p, u1, u2)


def _mid_fwd(z, y1, nw_post, nw_pre, modc, modx):
    t = y1.shape[0]

    def body(zc_ref, zx_ref, y_ref, wpo_ref, wpr_ref, mc_ref, mx_ref, z1_ref, h_ref):
        is_ctx = pl.program_id(0) < NCT
        y = y_ref[...].astype(F32)
        z1 = _z_tile(zc_ref, zx_ref, is_ctx) + _mod_row(mc_ref, mx_ref, 2, is_ctx) * (y * _rstd(y) * wpo_ref[...])
        z1_ref[...] = z1
        n = z1 * _rstd(z1) * wpr_ref[...]
        h = n * (1.0 + _mod_row(mc_ref, mx_ref, 4, is_ctx)) + _mod_row(mc_ref, mx_ref, 3, is_ctx)
        h_ref[...] = h.astype(BF16)

    return pl.pallas_call(
        body, name="mid_fwd", grid=(t // TR,),
        in_specs=_z_specs() + [_row(D), _full((1, D)), _full((1, D)), _full((8, D)), _full((8, D))],
        out_specs=[_row(D), _row(D)],
        out_shape=[jax.ShapeDtypeStruct((t, D), F32), jax.ShapeDtypeStruct((t, D), BF16)],
        compiler_params=_cp("parallel"),
    )(*z, y1, nw_post, nw_pre, modc, modx)


def _final(z1, y2, target, nw, modc, modx):
    t = z1.shape[0]

    def body(z1_ref, y_ref, tg_ref, w_ref, mc_ref, mx_ref, dz_ref, dy_ref, loss_ref, sm_ref):
        i = pl.program_id(0)
        is_ctx = i < NCT

        @pl.when(i == 0)
        def _():
            loss_ref[...] = jnp.zeros_like(loss_ref)
            sm_ref[...] = jnp.zeros_like(sm_ref)

        g = _mod_row(mc_ref, mx_ref, 5, is_ctx)
        y = y_ref[...].astype(F32)
        r = _rstd(y)
        w = w_ref[...]
        yr = y * r
        n = yr * w
        e = z1_ref[...] + g * n - tg_ref[...]
        lat = jnp.where(is_ctx, 0.0, 1.0)
        loss_ref[...] += lat * _colsum(e * e)
        dz = e * (lat / D)
        dz_ref[...] = dz
        _acc_mod(sm_ref, 0, is_ctx, _colsum(dz * n))
        dn = dz * g
        _acc_row(sm_ref, 2, _colsum(dn * yr))
        dy_ref[...] = _rms_bwd(dn * w, y, r).astype(BF16)

    return pl.pallas_call(
        body, name="final", grid=(t // TR,),
        in_specs=[_row(D), _row(D), pl.BlockSpec((TR, D), lambda i: (jnp.maximum(i - NCT, 0), 0)),
                  _full((1, D)), _full((8, D)), _full((8, D))],
        out_specs=[_row(D), _row(D), _full((1, D)), _full((8, D))],
        out_shape=[jax.ShapeDtypeStruct((t, D), F32), jax.ShapeDtypeStruct((t, D), BF16),
                   jax.ShapeDtypeStruct((1, D), F32), jax.ShapeDtypeStruct((8, D), F32)],
        compiler_params=_cp("arbitrary"),
    )(z1, y2, target, nw, modc, modx)


def _mid_bwd(dh2, dz, z1, y1, nw_post, nw_pre, modc, modx):
    t = z1.shape[0]

    def body(dh_ref, dz_ref, z1_ref, y_ref, wpo_ref, wpr_ref, mc_ref, mx_ref, dzo_ref, dy_ref, sm_ref):
        i = pl.program_id(0)
        is_ctx = i < NCT

        @pl.when(i == 0)
        def _():
            sm_ref[...] = jnp.zeros_like(sm_ref)

        dh = dh_ref[...].astype(F32)
        z1 = z1_ref[...]
        r = _rstd(z1)
        zr = z1 * r
        wpr = wpr_ref[...]
        n = zr * wpr
        _acc_mod(sm_ref, 0, is_ctx, _colsum(dh))
        _acc_mod(sm_ref, 2, is_ctx, _colsum(dh * n))
        dn = dh * (1.0 + _mod_row(mc_ref, mx_ref, 4, is_ctx))
        _acc_row(sm_ref, 6, _colsum(dn * zr))
        dz1 = dz_ref[...] + _rms_bwd(dn * wpr, z1, r)
        dzo_ref[...] = dz1
        y = y_ref[...].astype(F32)
        r1 = _rstd(y)
        yr = y * r1
        wpo = wpo_ref[...]
        g = _mod_row(mc_ref, mx_ref, 2, is_ctx)
        _acc_mod(sm_ref, 4, is_ctx, _colsum(dz1 * (yr * wpo)))
        dn1 = dz1 * g
        _acc_row(sm_ref, 7, _colsum(dn1 * yr))
        dy_ref[...] = _rms_bwd(dn1 * wpo, y, r1).astype(BF16)

    return pl.pallas_call(
        body, name="mid_bwd", grid=(t // TR,),
        in_specs=[_row(D)] * 4 + [_full((1, D)), _full((1, D)), _full((8, D)), _full((8, D))],
        out_specs=[_row(D), _row(D), _full((8, D))],
        out_shape=[jax.ShapeDtypeStruct((t, D), F32), jax.ShapeDtypeStruct((t, D), BF16),
                   jax.ShapeDtypeStruct((8, D), F32)],
        compiler_params=_cp("arbitrary"),
    )(dh2, dz, z1, y1, nw_post, nw_pre, modc, modx)


def _pre_bwd(dh1, dz, z, nw, modc, modx):
    t = dh1.shape[0]

    def body(dh_ref, dz_ref, zc_ref, zx_ref, w_ref, mc_ref, mx_ref, dzo_ref, sm_ref):
        i = pl.program_id(0)
        is_ctx = i < NCT

        @pl.when(i == 0)
        def _():
            sm_ref[...] = jnp.zeros_like(sm_ref)

        dh = dh_ref[...].astype(F32)
        x = _z_tile(zc_ref, zx_ref, is_ctx)
        r = _rstd(x)
        xr = x * r
        w = w_ref[...]
        _acc_mod(sm_ref, 0, is_ctx, _colsum(dh))
        _acc_mod(sm_ref, 2, is_ctx, _colsum(dh * (xr * w)))
        dn = dh * (1.0 + _mod_row(mc_ref, mx_ref, 1, is_ctx))
        _acc_row(sm_ref, 4, _colsum(dn * xr))
        dzo_ref[...] = dz_ref[...] + _rms_bwd(dn * w, x, r)

    return pl.pallas_call(
        body, name="pre_bwd", grid=(t // TR,),
        in_specs=[_row(D)] * 2 + _z_specs() + [_full((1, D)), _full((8, D)), _full((8, D))],
        out_specs=[pl.BlockSpec((TR, D), lambda i: (jnp.maximum(i - NCT, 0), 0)), _full((8, D))],
        out_shape=[jax.ShapeDtypeStruct((t - CTX, D), F32), jax.ShapeDtypeStruct((8, D), F32)],
        compiler_params=_cp("arbitrary"),
    )(dh1, dz, *z, nw, modc, modx)


def _merge_bwd(dm, p, u1, u2):
    t = dm.shape[0]

    def body(dm_ref, a0, a1, a2, b0, b1, b2, u1_ref, u2_ref, du1_ref, du2_ref, dg_ref):
        dm_ = dm_ref[...].astype(F32)
        s1 = _sig(_gate_window((a0, a1, a2)))
        s2 = _sig(_gate_window((b0, b1, b2)))
        du1_ref[...] = (dm_ * s1).astype(BF16)
        du2_ref[...] = (dm_ * s2).astype(BF16)
        dg_ref[:, :GW] = (dm_ * u1_ref[...].astype(F32) * s1 * (1.0 - s1)).astype(BF16)
        dg_ref[:, GW:] = (dm_ * u2_ref[...].astype(F32) * s2 * (1.0 - s2)).astype(BF16)

    return pl.pallas_call(
        body, name="merge_bwd", grid=(t // TR,),
        in_specs=[_row(GW)] + _gate_window_specs(GATE_HG0) + _gate_window_specs(GATE_GLA0) + [_row(GW), _row(GW)],
        out_specs=[_row(GW), _row(GW), _row(2 * GW)],
        out_shape=[jax.ShapeDtypeStruct((t, GW), BF16), jax.ShapeDtypeStruct((t, GW), BF16),
                   jax.ShapeDtypeStruct((t, 2 * GW), BF16)],
        compiler_params=_cp("parallel"),
    )(dm, p, p, p, p, p, p, u1, u2)


def _post_bwd(dy_hg, dy_gla, o_fw, o_bw, p, onw):
    t = o_fw.shape[0]

    def body(d1_ref, d2_ref, of_ref, ob_ref, g1_ref, g2_ref, w_ref, do_ref, dg_ref, sm_ref):
        @pl.when(pl.program_id(0) == 0)
        def _():
            sm_ref[...] = jnp.zeros_like(sm_ref)

        for h in range(NH):
            sl = slice(h * DH, (h + 1) * DH)
            gs = slice((h % (NH // 2)) * DH, (h % (NH // 2) + 1) * DH)
            g_ref, d_ref = (g1_ref, d1_ref) if h < NH // 2 else (g2_ref, d2_ref)
            o = of_ref[:, sl] + ob_ref[:, sl]
            r = _rstd(o)
            orr = o * r
            w = w_ref[:, sl]
            gt = g_ref[:, gs].astype(F32)
            dy = d_ref[:, gs].astype(F32)
            dg_ref[:, sl] = (dy * (orr * w) * _dsilu(gt)).astype(BF16)
            dn = dy * _silu(gt)
            sm_ref[0:1, sl] += _colsum(dn * orr)
            do_ref[:, sl] = _rms_bwd(dn * w, o, r)

    return pl.pallas_call(
        body, name="post_bwd", grid=(t // TR,),
        in_specs=[_row(HW), _row(HW), _row(D), _row(D), _rowcol(HW, MAIN0 // HW + 4), _rowcol(HW, MAIN0 // HW + 8),
                  _full((1, D))],
        out_specs=[_row(D), _row(D), _full((8, D))],
        out_shape=[jax.ShapeDtypeStruct((t, D), F32), jax.ShapeDtypeStruct((t, D), BF16),
                   jax.ShapeDtypeStruct((8, D), F32)],
        compiler_params=_cp("arbitrary"),
    )(dy_hg, dy_gla, o_fw, o_bw, p, p, onw)


def _gates_bwd(p, hg_lb, wgk, bgk, dgm, dgo, dq_f, dq_b, dv_f, dv_b, dk_f, dk_b, dg_f, dg_b):
    t = p.shape[0]
    seg = lambda j: _rowcol(HW, MAIN0 // HW + j)

    def body(hq_ref, hf_ref, hb_ref, lr_ref, lb_ref, wgk_ref, bgk_ref, dgm_ref, dgo_ref,
             dqf_ref, dqb_ref, dvf_ref, dvb_ref, dkf_ref, dkb_ref, dgf_ref, dgb_ref,
             dp_ref, dlb_ref, dw_ref, db_ref):
        @pl.when(pl.program_id(0) == 0)
        def _():
            dlb_ref[...] = jnp.zeros_like(dlb_ref)
            dw_ref[...] = jnp.zeros_like(dw_ref)
            db_ref[...] = jnp.zeros_like(db_ref)

        c0 = MAIN0

        def put(j, val):
            dp_ref[:, c0 + j * HW:c0 + (j + 1) * HW] = val.astype(BF16)

        dq = dqf_ref[...].astype(F32) + dqb_ref[...].astype(F32)
        dv = dvf_ref[...].astype(F32) + dvb_ref[...].astype(F32)
        put(0, dq[:, :HW] * _dsilu(hq_ref[...].astype(F32)))
        put(1, dv[:, :HW])
        put(5, dq[:, HW:] * (DH ** -0.5))
        put(7, dv[:, HW:])
        put(6, dkf_ref[:, HW:].astype(F32) + dkb_ref[:, HW:].astype(F32))
        dp_ref[:, c0 + 4 * HW:c0 + 5 * HW] = dgo_ref[:, :HW]
        dp_ref[:, c0 + 8 * HW:c0 + 9 * HW] = dgo_ref[:, HW:]
        lr = lr_ref[...].astype(BF16)
        xg = _dot(lr, wgk_ref[...], NN) + bgk_ref[...]
        dxg = []
        for d, (raw_ref, dk_ref, dg_ref) in enumerate(((hf_ref, dkf_ref, dgf_ref), (hb_ref, dkb_ref, dgb_ref))):
            lbd = _hg_lb(lb_ref, d)
            s = _sig(raw_ref[...].astype(F32))
            f = lbd + (1.0 - lbd) * s
            df = dg_ref[:, :HW] / f - dk_ref[:, :HW].astype(F32)
            put(2 + d, df * (1.0 - lbd) * s * (1.0 - s))
            dlb_ref[d:d + 1, :] += _colsum(df * (1.0 - s)) * (lbd * (1.0 - lbd))
            dxg.append(dg_ref[:, HW:] * (1.0 / GLA_NORM) * _sig(-xg[:, d * HW:(d + 1) * HW]))
        dxg = jnp.concatenate(dxg, axis=1)
        db_ref[0:1, :] += _colsum(dxg)
        dxg_b = dxg.astype(BF16)
        dw_ref[...] += _dot(lr, dxg_b, TN)
        dlr = _dot(dxg_b, wgk_ref[...], NT)
        dp_ref[:, LR0:LR0 + DH] = (dlr + dgm_ref[:, :DH].astype(F32)).astype(BF16)
        dp_ref[:, LR0 + DH:GATE_GLA0] = dgm_ref[:, DH:D]
        dp_ref[:, GATE_GLA0:GATE_GLA0 + DH] = dgm_ref[:, D:GW] + dgm_ref[:, GW:GW + DH]
        dp_ref[:, GATE_GLA0 + DH:GATE_GLA0 + GW] = dgm_ref[:, GW + DH:]
        dp_ref[:, GATE_GLA0 + GW:] = jnp.zeros((TR, W_IN_COLS - GATE_GLA0 - GW), BF16)

    return pl.pallas_call(
        body, name="gates_bwd", grid=(t // TR,),
        in_specs=[seg(0), seg(2), seg(3), _rowcol(DH, LR0 // DH), _full((2, 2, HW)), _full((DH, D)), _full((1, D)),
                  _row(2 * GW), _row(D)] + [_row(D)] * 8,
        out_specs=[_row(W_IN_COLS), _full((8, HW)), _full((DH, D)), _full((8, D))],
        out_shape=[jax.ShapeDtypeStruct((t, W_IN_COLS), BF16), jax.ShapeDtypeStruct((8, HW), F32),
                   jax.ShapeDtypeStruct((DH, D), F32), jax.ShapeDtypeStruct((8, D), F32)],
        compiler_params=_cp("arbitrary"),
    )(p, p, p, p, hg_lb, wgk, bgk, dgm, dgo, dq_f, dq_b, dv_f, dv_b, dk_f, dk_b, dg_f, dg_b)


def _scan_consts(rev):
    r = lax.broadcasted_iota(jnp.int32, (CHUNK, CHUNK), 0)
    u = lax.broadcasted_iota(jnp.int32, (CHUNK, CHUNK), 1)
    rp = lax.broadcasted_iota(jnp.int32, (CHUNK, 1), 0)
    if rev:
        r, u, rp = CHUNK - 1 - r, CHUNK - 1 - u, CHUNK - 1 - rp
    tri = jnp.where(u <= r, 1.0, 0.0).astype(F32)
    tri_t = jnp.where(r <= u, 1.0, 0.0).astype(F32)
    lv = []
    for b in LEVELS:
        sh = b.bit_length() - 1
        pair = ((r >> sh) == (u >> sh) + 1) & (((u >> sh) & 1) == 0)
        pair_t = ((u >> sh) == (r >> sh) + 1) & (((r >> sh) & 1) == 0)
        tside = ((rp >> sh) & 1) == 1
        lv.append((pair, pair_t, tside, jnp.where(tside, 1.0, -1.0).astype(F32)))
    bd = LEVELS[-1].bit_length() - 1
    diag = ((r >> bd) == (u >> bd)) & (u <= r)
    diag_t = ((r >> bd) == (u >> bd)) & (r <= u)
    return tri, tri_t, lv, diag, diag_t


def _row_of(pos, rev):
    return CHUNK - 1 - pos if rev else pos


def _chunk_terms(cum, b_scr, consts, rev):
    _, _, lv, _, _ = consts
    terms = []
    for b, (_, _, _, sgn) in zip(LEVELS, lv):
        pieces = []
        for j in range(CHUNK // (2 * b)):
            row = _row_of(2 * b * j + b - 1, rev)
            pieces.append(jnp.broadcast_to(b_scr[row:row + 1, :], (2 * b, DH)))
        if rev:
            pieces = pieces[::-1]
        bnd = pieces[0] if len(pieces) == 1 else jnp.concatenate(pieces, axis=0)
        terms.append(jnp.exp((cum - bnd) * sgn))
    b = LEVELS[-1]
    pieces = []
    for j in range(CHUNK // b):
        if j == 0:
            pieces.append(jnp.zeros((b, DH), F32))
        else:
            row = _row_of(b * j - 1, rev)
            pieces.append(jnp.broadcast_to(b_scr[row:row + 1, :], (b, DH)))
    if rev:
        pieces = pieces[::-1]
    start = jnp.concatenate(pieces, axis=0)
    wq = jnp.exp(jnp.minimum(cum - start, 0.0))
    wk = jnp.exp(jnp.minimum(start - cum, EXP_CLAMP))
    terms.append((wq, wk))
    return terms


def _run_staged(units):
    live = list(units)
    while live:
        nxt = []
        for u in live:
            try:
                next(u)
                nxt.append(u)
            except StopIteration:
                pass
        live = nxt


SCAN_TB = 256
SCAN_CB = SCAN_TB // CHUNK


def _block_order(i, ntb, rev):
    nctx = CTX // SCAN_TB
    if not rev:
        return i
    return jnp.where(i < nctx, nctx - 1 - i, ntb - 1 - (i - nctx))


def _chunk_in_block(j, rev):
    return SCAN_CB - 1 - j if rev else j


def _scan_fwd(q, k, v, g, rev):
    t = q.shape[0]
    nc = t // CHUNK
    hpb = SCAN_HEADS_FWD

    def body(q_ref, k_ref, v_ref, g_ref, o_ref, st_ref, s_scr, b_scr):
        consts = _scan_consts(rev)
        _, _, lv, diag, _ = consts
        masks = [lvl[0] for lvl in lv] + [diag]

        @pl.when(pl.program_id(1) == 0)
        def _():
            s_scr[...] = jnp.zeros_like(s_scr)

        tri = consts[0]
        state = {hh: s_scr[hh] for hh in range(hpb)}

        def unit(hh, j):
            sl = slice(hh * DH, (hh + 1) * DH)
            c = _chunk_in_block(j, rev)
            rows = slice(c * CHUNK, (c + 1) * CHUNK)
            b_ref = b_scr.at[hh * SCAN_CB + j]
            qc, kc, vc, gc = q_ref[rows, sl], k_ref[rows, sl], v_ref[rows, sl], g_ref[rows, sl]
            cum = _split_dot(tri, gc)
            b_ref[...] = cum
            yield
            terms = _chunk_terms(cum, b_ref, consts, rev)
            qf, kf = qc.astype(F32), kc.astype(F32)
            xs = [(jnp.where(tside, qf, kf) * w).astype(BF16) for w, (_, _, tside, _) in zip(terms[:-1], lv)]
            qd, kd = (qf * terms[-1][0]).astype(BF16), (kf * terms[-1][1]).astype(BF16)
            tot = _colsum(gc)
            qe = (qf * jnp.exp(cum)).astype(BF16)
            ke = (kf * jnp.exp(tot - cum)).astype(BF16)
            vb = vc.astype(BF16)
            yield
            scs = [_dot(x, x, NT) for x in xs] + [_dot(qd, kd, NT)]
            kv = _dot(vb, ke, TN)
            yield
            a = jnp.zeros((CHUNK, CHUNK), F32)
            for sc, m in zip(scs, masks):
                a = a + jnp.where(m, sc, 0.0)
            o_intra = _dot(a.astype(BF16), vb, NN)
            yield
            st = state[hh]
            st_ref[hh, c] = st
            o_ref[rows, sl] = o_intra + _dot(qe, st.astype(BF16), NT)
            state[hh] = st * jnp.exp(tot) + kv
            yield

        _run_staged([unit(hh, j) for hh in range(hpb) for j in range(SCAN_CB)])
        for hh in range(hpb):
            s_scr[hh] = state[hh]

    ntb = t // SCAN_TB
    col = pl.BlockSpec((SCAN_TB, hpb * DH), lambda h, i: (_block_order(i, ntb, rev), h))
    return pl.pallas_call(
        body, name="scan_fwd_" + ("bw" if rev else "fw"), grid=(NH // hpb, ntb),
        in_specs=[col] * 4,
        out_specs=[col, pl.BlockSpec((hpb, SCAN_CB, DH, DH), lambda h, i: (h, _block_order(i, ntb, rev), 0, 0))],
        out_shape=[jax.ShapeDtypeStruct((t, D), F32), jax.ShapeDtypeStruct((NH, nc, DH, DH), F32)],
        scratch_shapes=[pltpu.VMEM((hpb, DH, DH), F32), pltpu.VMEM((hpb * SCAN_CB, CHUNK, DH), F32)],
        compiler_params=_cp("parallel", "arbitrary"),
    )(q, k, v, g)


def _scan_bwd(q, k, v, g, do, states, rev):
    t = q.shape[0]
    nc = t // CHUNK
    hpb = SCAN_HEADS_BWD

    def body(q_ref, k_ref, v_ref, g_ref, do_ref, st_ref, dq_ref, dk_ref, dv_ref, dg_ref, ds_scr, b_scr):
        consts = _scan_consts(rev)
        _, tri_t, lv, diag, diag_t = consts
        masks = [(lvl[0], lvl[1]) for lvl in lv] + [(diag, diag_t)]
        @pl.when(pl.program_id(1) == 0)
        def _():
            ds_scr[...] = jnp.zeros_like(ds_scr)

        tri = consts[0]
        dstate = {hh: ds_scr[hh] for hh in range(hpb)}

        def unit(hh, jj):
            sl = slice(hh * DH, (hh + 1) * DH)
            c = _chunk_in_block(SCAN_CB - 1 - jj, rev)
            rows = slice(c * CHUNK, (c + 1) * CHUNK)
            b_ref = b_scr.at[hh * SCAN_CB + jj]
            qc, kc, vc, gc = q_ref[rows, sl], k_ref[rows, sl], v_ref[rows, sl], g_ref[rows, sl]
            dob = do_ref[rows, sl].astype(BF16)
            vb = vc.astype(BF16)
            cum = _split_dot(tri, gc)
            b_ref[...] = cum
            da = _dot(dob, vb, NT)
            da_t = _dot(vb, dob, NT)
            yield
            terms = _chunk_terms(cum, b_ref, consts, rev)
            qf, kf = qc.astype(F32), kc.astype(F32)
            xs = [(jnp.where(tside, qf, kf) * w).astype(BF16) for w, (_, _, tside, _) in zip(terms[:-1], lv)]
            wqd, wkd = terms[-1]
            qdb, kdb = (qf * wqd).astype(BF16), (kf * wkd).astype(BF16)
            tot = _colsum(gc)
            e_tot = jnp.exp(tot)
            e_b = jnp.exp(cum)
            e_t = jnp.exp(tot - cum)
            qeb = (qf * e_b).astype(BF16)
            keb = (kf * e_t).astype(BF16)
            dsym = [(jnp.where(m, da, 0.0) + jnp.where(m_t, da_t, 0.0)).astype(BF16) for m, m_t in masks[:-1]]
            dad = (jnp.where(diag, da, 0.0).astype(BF16), jnp.where(diag_t, da_t, 0.0).astype(BF16))
            yield
            sym = [_dot(x, x, NT) for x in xs]
            dxs = [_dot(d, x, NN) for d, x in zip(dsym, xs)]
            at_d = _dot(kdb, qdb, NT)
            dqt_d = _dot(dad[0], kdb, NN)
            dkt_d = _dot(dad[1], qdb, NN)
            qd = _dot(dob, qeb, TN)
            yield
            a_t = jnp.where(diag_t, at_d, 0.0)
            dq = dqt_d * wqd
            dk = dkt_d * wkd
            db = dqt_d * qdb.astype(F32) - dkt_d * kdb.astype(F32)
            for s, dx, x, w, (_, m_t, tside, sgn) in zip(sym, dxs, xs, terms[:-1], lv):
                a_t = a_t + jnp.where(m_t, s, 0.0)
                dxw = dx * w
                dq = dq + jnp.where(tside, dxw, 0.0)
                dk = dk + jnp.where(tside, 0.0, dxw)
                db = db + (dx * x.astype(F32)) * sgn
            dv_intra = _dot(a_t.astype(BF16), dob, NN)
            st = st_ref[hh, c]
            stb = st.astype(BF16)
            dqe = _dot(dob, stb, NN)
            yield
            dst = dstate[hh]
            dstb = dst.astype(BF16)
            dstate[hh] = dst * e_tot + qd
            dv_ref[rows, sl] = (dv_intra + _dot(keb, dstb, NT)).astype(BF16)
            dke = _dot(vb, dstb, NN)
            yield
            qe = qeb.astype(F32)
            ke = keb.astype(F32)
            dq_ref[rows, sl] = (dq + dqe * e_b).astype(BF16)
            dk_ref[rows, sl] = (dk + dke * e_t).astype(BF16)
            db = db + dqe * qe - dke * ke
            dtot = _colsum(dstb.astype(F32) * stb.astype(F32)) * e_tot + _colsum(dke * ke)
            dg_ref[rows, sl] = _split_dot(tri_t, db) + dtot
            yield

        _run_staged([unit(hh, jj) for hh in range(hpb) for jj in range(SCAN_CB)])
        for hh in range(hpb):
            ds_scr[hh] = dstate[hh]

    ntb = t // SCAN_TB
    blk = lambda i: _block_order(ntb - 1 - i, ntb, rev)
    col = pl.BlockSpec((SCAN_TB, hpb * DH), lambda h, i: (blk(i), h))
    out = jax.ShapeDtypeStruct((t, D), F32)
    outb = jax.ShapeDtypeStruct((t, D), BF16)
    return pl.pallas_call(
        body, name="scan_bwd_" + ("bw" if rev else "fw"), grid=(NH // hpb, ntb),
        in_specs=[col] * 5 + [pl.BlockSpec((hpb, SCAN_CB, DH, DH), lambda h, i: (h, blk(i), 0, 0))],
        out_specs=[col] * 4,
        out_shape=[outb] * 3 + [out],
        scratch_shapes=[pltpu.VMEM((hpb, DH, DH), F32), pltpu.VMEM((hpb * SCAN_CB, CHUNK, DH), F32)],
        compiler_params=_cp("parallel", "arbitrary"),
    )(q, k, v, g, do, states)


W_IN_GRAD_CHUNKS = (("a", (0, 512)), ("b", (0, 256)), ("b", (256, 512)))
W_IN_REF = 6688


def _layout_w_in(w):
    return jnp.pad(w, ((0, 0), (0, W_IN_COLS - W_IN_REF)))


def _unlayout_w_in(d):
    return d[:, :W_IN_REF]


W_IN_PAD = 896
W_IN_PIECE = 256


def _assemble_w_in(g):
    n, r, wp = g.shape
    tr = 256
    tiles = wp // DH

    def body(g_ref, o_ref):
        lane = lax.broadcasted_iota(jnp.int32, (tr, DH), 1)
        for t in range(W_IN_COLS // DH):
            acc = None
            for j in range(n):
                c = DH * t - W_IN_SHARD * j
                if c <= -DH or c >= W_IN_SHARD:
                    continue
                k, s = divmod(c, DH)
                lo = g_ref[j, :, k * DH:(k + 1) * DH] if 0 <= k < tiles else None
                hi = g_ref[j, :, (k + 1) * DH:(k + 2) * DH] if s and 0 <= k + 1 < tiles else None
                if s:
                    zero = jnp.zeros((tr, DH), g.dtype)
                    lo = zero if lo is None else pltpu.roll(lo, DH - s, 1)
                    hi = zero if hi is None else pltpu.roll(hi, DH - s, 1)
                    part = jnp.where(lane < DH - s, lo, hi)
                else:
                    part = lo
                acc = part if acc is None else acc + part
            o_ref[:, t * DH:(t + 1) * DH] = jnp.zeros((tr, DH), g.dtype) if acc is None else acc

    return pl.pallas_call(
        body, name="assemble_w_in", grid=(r // tr,),
        in_specs=[pl.BlockSpec((n, tr, wp), lambda i: (0, i, 0))],
        out_specs=pl.BlockSpec((tr, W_IN_COLS), lambda i: (i, 0)),
        out_shape=jax.ShapeDtypeStruct((r, W_IN_COLS), g.dtype),
        compiler_params=_cp("parallel"),
    )(g)


def _gate_cols(w):
    return jnp.pad(w, ((0, 0), (GOFF, GW - GOFF - D)))


def _gate_rows(w):
    return jnp.pad(w, ((GOFF, GW - GOFF - D), (0, 0)))


def _layout_wgk(w):
    r = w.shape[1]
    top = jnp.concatenate([w[0], jnp.zeros_like(w[0])], axis=1)
    bot = jnp.concatenate([jnp.zeros_like(w[1]), w[1]], axis=1)
    return jnp.concatenate([top, bot, jnp.zeros((DH - 2 * r, D), w.dtype)], axis=0)


def _unlayout_wgk(d, r=16):
    return jnp.stack([d[:r, :HW], d[r:2 * r, HW:]])


def _local_step(z, target, modc, modx, norms, onw, hg_lb, wgk, bgk, get_w_in, get_mix, get_ffn, send):
    n_pre1, n_post1, n_pre2, n_post2 = norms
    t = z[0].shape[0] + z[1].shape[0]
    tm = 1152 if t % 1152 == 0 else 256
    h1 = _prenorm(z, n_pre1, modc, modx, 0, 1, "prenorm1")
    w_in = get_w_in(h1)
    p = _matmul(h1, w_in, NN, BF16, "mm_in", t, 1024, D)
    q, v, k_f, k_b, g_f, g_b = _gates_fwd(p, hg_lb, wgk, bgk)
    o_f, st_f = _scan_fwd(q, k_f, v, g_f, False)
    o_b, st_b = _scan_fwd(q, k_b, v, g_b, True)
    y = _post_fwd(o_f, o_b, p, onw)
    w_br_hg, w_br_gla, w_out = get_mix(y)
    u1 = _matmul(y, w_br_hg, NN, BF16, "mm_br_hg", tm, GW, HW, a_off=0)
    u2 = _matmul(y, w_br_gla, NN, BF16, "mm_br_gla", tm, GW, HW, a_off=1)
    merged = _merge_fwd(p, u1, u2)
    y1 = _matmul(merged, w_out, NN, BF16, "mm_out", tm, 512, GW)
    z1, h2 = _mid_fwd(z, y1, n_post1, n_pre2, modc, modx)
    w_gu_t, w_down = get_ffn(h2)
    u, v_ff, act = _mm_gu_act(h2, w_gu_t[0], w_gu_t[1], "mm_gu", tm)
    y2 = _matmul(act, w_down, NN, BF16, "mm_down", t, 512, D_FF)
    dz, dy2, loss_vec, sm_final = _final(z1, y2, target, n_post2, modc, modx)
    du, dv_ff = _mm_down_dx_act(dy2, w_down, u, v_ff, "mm_down_dx", tm)
    d_w_down = _matmul(act, dy2, TN, BF16, "mm_down_dw", D_FF // 2, 1024, t)
    dh2 = _matmul((du, dv_ff), w_gu_t, NN, BF16, "mm_gu_dx", tm, 512, D_FF)
    d_w_gate_t = _matmul(du, h2, TN, BF16, "mm_gate_dw", D_FF // 2, 1024, t)
    d_w_up_t = _matmul(dv_ff, h2, TN, BF16, "mm_up_dw", D_FF // 2, 1024, t)
    dh2 = send(("w_down", "w_gate_t", "w_up_t"), (d_w_down, d_w_gate_t, d_w_up_t), dh2)
    dz, dy1, sm_mid = _mid_bwd(dh2, dz, z1, y1, n_post1, n_pre2, modc, modx)
    dmerged = _matmul(dy1, w_out, NT, BF16, "mm_out_dx", tm, GW, D)
    d_w_out = _matmul(merged, dy1, TN, BF16, "mm_out_dw", GW, 512, t)
    du1, du2, dgm = _merge_bwd(dmerged, p, u1, u2)
    dy_hg = _matmul(du1, w_br_hg, NT, BF16, "mm_br_hg_dx", tm, HW, GW)
    dy_gla = _matmul(du2, w_br_gla, NT, BF16, "mm_br_gla_dx", tm, HW, GW)
    d_w_br_hg = _matmul(y, du1, TN, BF16, "mm_br_hg_dw", HW, GW, t, a_off=0, m_out=HW)
    d_w_br_gla = _matmul(y, du2, TN, BF16, "mm_br_gla_dw", HW, GW, t, a_off=1, m_out=HW)
    dy_hg = send(("w_out", "w_br_hg", "w_br_gla"), (d_w_out, d_w_br_hg, d_w_br_gla), dy_hg)
    do, dgo, sm_post = _post_bwd(dy_hg, dy_gla, o_f, o_b, p, onw)
    dq_f, dk_f, dv_f, dg_f = _scan_bwd(q, k_f, v, g_f, do, st_f, False)
    dq_b, dk_b, dv_b, dg_b = _scan_bwd(q, k_b, v, g_b, do, st_b, True)
    dp, d_lb, d_wgk, d_bgk = _gates_bwd(p, hg_lb, wgk, bgk, dgm, dgo, dq_f, dq_b, dv_f, dv_b, dk_f, dk_b, dg_f, dg_b)
    d_w_in_a = _matmul(h1, dp, TN, BF16, "mm_in_dw_a", 512, 1024, t, a_off=0, m_out=D // 2)
    dp = send(("w_in_a",), (d_w_in_a,), dp)
    d_w_in_b = _matmul(h1, dp, TN, BF16, "mm_in_dw_b", 512, 1024, t, a_off=1, m_out=D // 2)
    dp = send(("w_in_b",), (d_w_in_b,), dp)
    dh1 = _matmul(dp, w_in, NT, BF16, "mm_in_dx", tm, 512, W_IN_COLS // 2)
    grad_x, sm_pre = _pre_bwd(dh1, dz, z, n_pre1, modc, modx)
    return dict(loss_vec=loss_vec, grad_x=grad_x, sm_final=sm_final, sm_mid=sm_mid, sm_post=sm_post, sm_pre=sm_pre,
                d_lb=d_lb, d_wgk=d_wgk, d_bgk=d_bgk)


MESH = pl.DeviceIdType.MESH
ANY = pl.BlockSpec(memory_space=pl.ANY)
N_REL = N_DEV - 1


def _place():
    return lax.axis_index("x"), lax.axis_index("y"), lax.axis_index("c")


def _slot(p):
    return 4 * p[0] + 2 * p[1] + p[2]


def _all_gather(arrays, name):
    n = len(arrays)

    def body(*refs):
        ins, outs = refs[:n], refs[n:2 * n]
        send_sems, recv_sems, local_sems = refs[2 * n:]
        x, y, c = _place()
        me, sibling = (x, y, c), (x, y, 1 - c)
        chips = [(1 - x, y), (x, 1 - y), (1 - x, 1 - y)]

        def copy(a, k, block, to, src=None):
            dst = outs[a].at[_slot(block)]
            return pltpu.make_async_remote_copy(
                src_ref=dst if src is None else src, dst_ref=dst,
                send_sem=send_sems.at[N_REL * a + k], recv_sem=recv_sems.at[N_REL * a + k],
                device_id=to, device_id_type=MESH)

        mine = [pltpu.make_async_copy(ins[a], outs[a].at[_slot(me)], local_sems.at[a]) for a in range(n)]
        for cp in mine:
            cp.start()
        first = []
        for a in range(n):
            first.append(copy(a, 0, me, sibling, src=ins[a]))
            first += [copy(a, 1 + j, me, (*chip, c), src=ins[a]) for j, chip in enumerate(chips)]
        for cp in first:
            cp.start()
        passed = []
        for j, chip in enumerate(chips):
            for a in range(n):
                copy(a, 1 + j, (*chip, c), me).wait_recv()
                fwd = copy(a, 4 + j, (*chip, c), sibling)
                fwd.start()
                passed.append(fwd)
        for a in range(n):
            copy(a, 0, sibling, me).wait_recv()
        for j, chip in enumerate(chips):
            for a in range(n):
                copy(a, 4 + j, (*chip, 1 - c), me).wait_recv()
        for cp in first + passed:
            cp.wait_send()
        for cp in mine:
            cp.wait()

    return pl.pallas_call(
        body, name=name,
        in_specs=[ANY] * n, out_specs=[ANY] * n,
        out_shape=[jax.ShapeDtypeStruct((N_DEV,) + a.shape, a.dtype) for a in arrays],
        scratch_shapes=[pltpu.SemaphoreType.DMA((N_REL * n,)), pltpu.SemaphoreType.DMA((N_REL * n,)),
                        pltpu.SemaphoreType.DMA((n,))],
    )(*arrays)


def _exchange(arrays, name):
    n = len(arrays)

    def body(*refs):
        ins, outs = refs[:n], refs[n:2 * n]
        send_sems, recv_sems, local_sems = refs[2 * n:]
        x, y, c = _place()
        me = _slot((x, y, c))
        mine = [pltpu.make_async_copy(ins[a].at[me], outs[a].at[me], local_sems.at[a]) for a in range(n)]
        for cp in mine:
            cp.start()
        copies = []
        for a in range(n):
            for k in range(1, N_DEV):
                flip = lambda v, bit: 1 - v if bit else v
                peer = (flip(x, k & 4), flip(y, k & 2), flip(c, k & 1))
                copies.append(pltpu.make_async_remote_copy(
                    src_ref=ins[a].at[_slot(peer)], dst_ref=outs[a].at[me],
                    send_sem=send_sems.at[N_REL * a + k - 1], recv_sem=recv_sems.at[N_REL * a + k - 1],
                    device_id=peer, device_id_type=MESH))
                copies[-1].start()
        i = 0
        for a in range(n):
            for k in range(1, N_DEV):
                flip = lambda v, bit: 1 - v if bit else v
                peer = (flip(x, k & 4), flip(y, k & 2), flip(c, k & 1))
                pltpu.make_async_remote_copy(
                    src_ref=ins[a].at[_slot(peer)], dst_ref=outs[a].at[_slot(peer)],
                    send_sem=send_sems.at[N_REL * a + k - 1], recv_sem=recv_sems.at[N_REL * a + k - 1],
                    device_id=peer, device_id_type=MESH).wait_recv()
                i += 1
        for cp in copies:
            cp.wait_send()
        for cp in mine:
            cp.wait()

    return pl.pallas_call(
        body, name=name,
        in_specs=[ANY] * n, out_specs=[ANY] * n,
        out_shape=[jax.ShapeDtypeStruct(a.shape, a.dtype) for a in arrays],
        scratch_shapes=[pltpu.SemaphoreType.DMA((N_REL * n,)), pltpu.SemaphoreType.DMA((N_REL * n,)),
                        pltpu.SemaphoreType.DMA((n,))],
    )(*arrays)


HBM = pl.BlockSpec(memory_space=pltpu.HBM)
SEM = pl.BlockSpec(memory_space=pltpu.SEMAPHORE)
EFFECT = pltpu.SideEffectType.DATAFLOW_SIDE_EFFECTING


def _peer_of(x, y, c, k):
    flip = lambda v, bit: 1 - v if bit else v
    return flip(x, k & 4), flip(y, k & 2), flip(c, k & 1)


def _view_whole(src, slot):
    return src


def _view_near(src, slot):
    return src


_view_near.peers = (1, 2, 4, 6)


def _view_near_rows(rows):
    def view(src, slot):
        return src.at[pl.ds(rows[0], rows[1] - rows[0])]
    view.peers = _view_near.peers
    view.land = lambda land, slot: land.at[slot, pl.ds(rows[0], rows[1] - rows[0])]
    return view


def _view_block(src, slot):
    return src.at[slot]


W_IN_SHARD = W_IN_REF // N_DEV


def _view_window(rows):
    def view(src, slot):
        col0 = pl.multiple_of((W_IN_SHARD * slot // DH) * DH, DH)
        return src.at[pl.ds(rows[0], rows[1] - rows[0]), pl.ds(col0, D)]
    return view


def _split_copies(view, srcs, lands, send_sems, recv_sems, local_sems):
    x, y, c = _place()
    me = _slot((x, y, c))
    into = getattr(view, "land", lambda land, slot: land.at[slot])
    local, sends, waits = [], [], []
    for a, (src, land) in enumerate(zip(srcs, lands)):
        local.append(pltpu.make_async_copy(view(src, me), into(land, me), local_sems.at[a]))
        for k in getattr(view, "peers", range(1, N_DEV)):
            peer = _peer_of(x, y, c, k)
            mine = view(src, _slot(peer))
            sems = dict(send_sem=send_sems.at[N_REL * a + k - 1], recv_sem=recv_sems.at[N_REL * a + k - 1],
                        device_id=peer, device_id_type=MESH)
            sends.append(pltpu.make_async_remote_copy(src_ref=mine, dst_ref=into(land, me), **sems))
            waits.append(pltpu.make_async_remote_copy(src_ref=mine, dst_ref=into(land, _slot(peer)), **sems))
    return local, sends, waits


def _split_start(view, land_shapes, srcs, name, after, lands=None):
    n = len(srcs)
    if lands is None:
        lands = [lax.empty(shp, s.dtype) for shp, s in zip(land_shapes, srcs)]

    def body(*refs):
        src_refs, land_refs = refs[:n], refs[n:2 * n]
        send_sems, recv_sems, local_sems = refs[2 * n + 1:2 * n + 4]
        token = refs[-1]
        local, sends, _ = _split_copies(view, src_refs, land_refs, send_sems, recv_sems, local_sems)
        for cp in local + sends:
            cp.start()
        token[...] = jnp.zeros_like(token)

    hbm = lambda a: pltpu.with_memory_space_constraint(a, pltpu.HBM)
    out = pl.pallas_call(
        body, name=name,
        out_shape=(pltpu.SemaphoreType.DMA((N_REL * n,)), pltpu.SemaphoreType.DMA((N_REL * n,)),
                   pltpu.SemaphoreType.DMA((n,)),
                   *[pltpu.HBM(s.shape, s.dtype) for s in srcs], *[pltpu.HBM(l.shape, l.dtype) for l in lands],
                   jax.ShapeDtypeStruct((8, DH), F32)),
        in_specs=[HBM] * (2 * n) + [ANY],
        out_specs=(SEM, SEM, SEM, *([HBM] * (2 * n)), pl.BlockSpec(memory_space=pltpu.VMEM)),
        input_output_aliases={i: 3 + i for i in range(2 * n)},
        compiler_params=pltpu.CompilerParams(has_side_effects=EFFECT),
    )(*[hbm(s) for s in srcs], *[hbm(l) for l in lands], after)
    handle = dict(view=view, n=n, sems=out[:3], srcs=list(out[3:3 + n]), lands=list(out[3 + n:3 + 2 * n]))
    return handle, out[-1]


def _split_wait(handle, name, after, srcs=None, lands=None):
    view, n, sems = handle["view"], handle["n"], handle["sems"]
    srcs = handle["srcs"] if srcs is None else srcs
    lands = handle["lands"] if lands is None else lands
    afters = list(after) if isinstance(after, (list, tuple)) else [after]

    def body(*refs):
        src_refs, land_refs = refs[:n], refs[n:2 * n]
        send_sems, recv_sems, local_sems = refs[2 * n:2 * n + 3]
        local, _, waits = _split_copies(view, src_refs, land_refs, send_sems, recv_sems, local_sems)
        for cp in waits:
            cp.wait_send()
            cp.wait_recv()
        for cp in local:
            cp.wait()

    out = pl.pallas_call(
        body, name=name,
        out_shape=(*[pltpu.HBM(s.shape, s.dtype) for s in srcs], *[pltpu.HBM(l.shape, l.dtype) for l in lands]),
        in_specs=[HBM] * (2 * n) + [SEM, SEM, SEM] + [ANY] * len(afters),
        out_specs=tuple([HBM] * (2 * n)),
        input_output_aliases={i: i for i in range(2 * n)},
        compiler_params=pltpu.CompilerParams(has_side_effects=EFFECT),
    )(*srcs, *lands, *sems, *afters)
    handle["srcs"] = list(out[:n])
    return list(out[n:])


def _tie(x, token, name):
    def body(x_ref, t_ref, o_ref):
        pass

    return pl.pallas_call(
        body, name=name, out_shape=jax.ShapeDtypeStruct(x.shape, x.dtype),
        in_specs=[ANY, ANY], out_specs=ANY, input_output_aliases={0: 0},
    )(x, token)


def _forward_to_sibling(land, name):
    def body(land_ref, out_ref, send_sems, recv_sems):
        x, y, c = _place()
        sibling = (x, y, 1 - c)
        chips = [(1 - x, y), (x, 1 - y), (1 - x, 1 - y)]

        def copy(j, core):
            blk = _slot((*chips[j], core))
            return pltpu.make_async_remote_copy(src_ref=land_ref.at[blk], dst_ref=out_ref.at[blk],
                                                send_sem=send_sems.at[j], recv_sem=recv_sems.at[j],
                                                device_id=sibling, device_id_type=MESH)

        sends = [copy(j, c) for j in range(3)]
        for cp in sends:
            cp.start()
        for j in range(3):
            copy(j, 1 - c).wait_recv()
        for cp in sends:
            cp.wait_send()

    return pl.pallas_call(
        body, name=name, in_specs=[ANY], out_specs=ANY, input_output_aliases={0: 0},
        out_shape=jax.ShapeDtypeStruct(land.shape, land.dtype),
        scratch_shapes=[pltpu.SemaphoreType.DMA((3,)), pltpu.SemaphoreType.DMA((3,))],
    )(land)


def _mod_fwd(a, w, b):
    def body(a_ref, w_ref, b_ref, o_ref):
        o_ref[...] = _dot(_silu(a_ref[...]), w_ref[...], NN, precision=HI) + b_ref[...]

    return pl.pallas_call(
        body, name="mod_fwd", out_shape=jax.ShapeDtypeStruct((a.shape[0], w.shape[1]), F32),
        compiler_params=pltpu.CompilerParams(vmem_limit_bytes=VMEM_LIMIT),
    )(a, w, b)


def _mod_bwd(a, d, w):
    def body(a_ref, d_ref, w_ref, dw_ref, dc_ref):
        av = a_ref[...]
        dv = d_ref[...]
        dw_ref[...] = _dot(_silu(av), dv, TN, precision=HI)
        da = _dot(dv[0:8, :], w_ref[...], NT, precision=HI) * _dsilu(av[0:8, :])
        row = lax.broadcasted_iota(jnp.int32, da.shape, 0)
        dc_ref[...] = jnp.where(row == 0, da, 0.0)

    return pl.pallas_call(
        body, name="mod_bwd",
        out_shape=[jax.ShapeDtypeStruct(w.shape, F32), jax.ShapeDtypeStruct((8, w.shape[0]), F32)],
        compiler_params=pltpu.CompilerParams(vmem_limit_bytes=VMEM_LIMIT),
    )(a, d, w)


def _sum_devices(g):
    def body(g_ref, o_ref):
        acc = g_ref[0]
        for i in range(1, g.shape[0]):
            acc = acc + g_ref[i]
        o_ref[...] = acc

    return pl.pallas_call(body, name="sum_devices_%d" % g.shape[1],
                          out_shape=jax.ShapeDtypeStruct(g.shape[1:], F32))(g)


def _sum_windows(g, name):
    n, r, c = g.shape
    tr = 128

    def body(g_ref, o_ref):
        x, y, cc = _place()
        lane0 = (W_IN_SHARD * _slot((x, y, cc))) % DH
        acc = g_ref[0].astype(F32)
        for i in range(1, n):
            acc = acc + g_ref[i].astype(F32)
        o_ref[...] = pltpu.roll(acc, (c - lane0) % c, 1).T

    return pl.pallas_call(
        body, name=name, grid=(r // tr,),
        in_specs=[pl.BlockSpec((n, tr, c), lambda i: (0, i, 0))],
        out_specs=pl.BlockSpec((c, tr), lambda i: (0, i)),
        out_shape=jax.ShapeDtypeStruct((c, r), F32),
        compiler_params=_cp("parallel"),
    )(g)


def _adam_rows(r, c, n):
    budget = 6 * 1024 * 1024
    best = None
    for tr in range(16, r + 1, 16):
        if r % tr == 0 and tr * c * (2 * n + 28) <= budget:
            best = tr
    return best if best is not None else r


def _adamw(g, w, m, v, name):
    n, r, c = g.shape
    tr = _adam_rows(r, c, n)
    bc1 = 1.0 - ADAM_B1 ** ADAM_STEP
    bc2 = 1.0 - ADAM_B2 ** ADAM_STEP

    def body(g_ref, w_ref, m_ref, v_ref, go_ref, d_ref, mo_ref, vo_ref):
        grad = g_ref[0].astype(F32)
        for i in range(1, n):
            grad = grad + g_ref[i].astype(F32)
        go_ref[...] = grad
        m_new = ADAM_B1 * m_ref[...] + (1.0 - ADAM_B1) * grad
        v_new = ADAM_B2 * v_ref[...] + (1.0 - ADAM_B2) * (grad * grad)
        mo_ref[...] = m_new
        vo_ref[...] = v_new
        d_ref[...] = -ADAM_LR * ((m_new / bc1) / (jnp.sqrt(v_new / bc2) + ADAM_EPS) + ADAM_WD * w_ref[...])

    blk = pl.BlockSpec((tr, c), lambda i: (i, 0))
    out = jax.ShapeDtypeStruct((r, c), F32)
    return pl.pallas_call(
        body, name=name, grid=(r // tr,),
        in_specs=[pl.BlockSpec((n, tr, c), lambda i: (0, i, 0)), blk, blk, blk],
        out_specs=[blk] * 4, out_shape=[out] * 4,
        compiler_params=_cp("parallel"),
    )(g, w, m, v)


ADAM_ROWS3 = 168


def _adam_math(grad, w, m, v):
    bc1 = 1.0 - ADAM_B1 ** ADAM_STEP
    bc2 = 1.0 - ADAM_B2 ** ADAM_STEP
    m_new = ADAM_B1 * m + (1.0 - ADAM_B1) * grad
    v_new = ADAM_B2 * v + (1.0 - ADAM_B2) * (grad * grad)
    delta = -ADAM_LR * ((m_new / bc1) / (jnp.sqrt(v_new / bc2) + ADAM_EPS) + ADAM_WD * w)
    return delta, m_new, v_new


def _adamw_rows3(g, w3, m3, v3, name):
    r, _, c = w3.shape
    n = ADAM_ROWS3
    starts = list(range(0, r - n, n)) + [r - n]

    def body(g_hbm, w_hbm, m_hbm, v_hbm, go_hbm, d_hbm, mo_hbm, vo_hbm, gbuf, ibuf, obuf, in_sems, out_sems):
        def fetch(p):
            r0, slot = starts[p], p % 2
            g0 = (r0 // 8) * 8
            cps = [pltpu.make_async_copy(g_hbm.at[pl.ds(g0, n + 8)], gbuf.at[slot], in_sems.at[slot, 0])]
            cps += [pltpu.make_async_copy(h.at[pl.ds(r0, n), 0], ibuf.at[slot, k], in_sems.at[slot, 1 + k])
                    for k, h in enumerate((w_hbm, m_hbm, v_hbm))]
            for cp in cps:
                cp.start()
            return cps

        pending, outs = fetch(0), []
        for p, r0 in enumerate(starts):
            slot = p % 2
            nxt = fetch(p + 1) if p + 1 < len(starts) else []
            for cp in pending:
                cp.wait()
            grad = gbuf[slot, pl.ds(r0 - (r0 // 8) * 8, n), :]
            delta, m_new, v_new = _adam_math(grad, ibuf[slot, 0], ibuf[slot, 1], ibuf[slot, 2])
            for cp in outs:
                cp.wait()
            for k, val in enumerate((grad, delta, m_new, v_new)):
                obuf[slot, k] = val
            outs = [pltpu.make_async_copy(obuf.at[slot, k], h.at[pl.ds(r0, n), 0], out_sems.at[slot, k])
                    for k, h in enumerate((go_hbm, d_hbm, mo_hbm, vo_hbm))]
            for cp in outs:
                cp.start()
            pending = nxt
        for cp in outs:
            cp.wait()

    out = jax.ShapeDtypeStruct(w3.shape, F32)
    return pl.pallas_call(
        body, name=name, in_specs=[ANY] * 4, out_specs=[ANY] * 4, out_shape=[out] * 4,
        scratch_shapes=[pltpu.VMEM((2, n + 8, c), F32), pltpu.VMEM((2, 3, n, c), F32), pltpu.VMEM((2, 4, n, c), F32),
                        pltpu.SemaphoreType.DMA((2, 4)), pltpu.SemaphoreType.DMA((2, 4))],
        compiler_params=pltpu.CompilerParams(vmem_limit_bytes=VMEM_LIMIT),
    )(g, w3, m3, v3)


def kernel(x, c, ctx, c_ctx, w_mod, b_mod, norm_pre1, norm_post1, norm_pre2, norm_post2, w_in, hg_lb, hg_onorm, gla_w_gk, gla_b_gk, gla_onorm, w_br_hg, w_br_gla, w_out, w_ff_gate, w_ff_up, w_ff_down, loss_target, m_c_ctx, m_w_mod, m_b_mod, m_norm_pre1, m_norm_post1, m_norm_pre2, m_norm_post2, m_w_in, m_hg_lb, m_hg_onorm, m_gla_w_gk, m_gla_b_gk, m_gla_onorm, m_w_br_hg, m_w_br_gla, m_w_out, m_w_ff_gate, m_w_ff_up, m_w_ff_down, v_c_ctx, v_w_mod, v_b_mod, v_norm_pre1, v_norm_post1, v_norm_pre2, v_norm_post2, v_w_in, v_hg_lb, v_hg_onorm, v_gla_w_gk, v_gla_b_gk, v_gla_onorm, v_w_br_hg, v_w_br_gla, v_w_out, v_w_ff_gate, v_w_ff_up, v_w_ff_down):
    xi, yi, ci = lax.axis_index("x"), lax.axis_index("y"), lax.axis_index("c")
    me = 4 * xi + 2 * yi + ci
    t = CTX + x.shape[1]

    w_in_pieces, w_in_state = [], {}

    def start_w_in(i, after):
        rows = (i * W_IN_PIECE, (i + 1) * W_IN_PIECE)
        handle, tok = _split_start(_view_near_rows(rows), None, w_in_state["src"], "ag_w_in_start%d" % i, after,
                                   lands=w_in_state["land"])
        w_in_state.update(src=handle["srcs"], land=handle["lands"])
        w_in_pieces.append(handle)
        return tok

    tr_ = lambda a: jnp.swapaxes(a[0], 0, 1)
    w_in_bf = jnp.pad(w_in[0].astype(BF16), ((0, 0), (0, W_IN_PAD - W_IN_SHARD)))
    w_in_state.update(src=[w_in_bf], land=[lax.empty((N_DEV,) + w_in_bf.shape, BF16)])
    tok = start_w_in(0, c)
    c_all, lb_g, wgk_g, bgk_g = _all_gather([_tie(c, tok, "tie_c"), hg_lb, gla_w_gk[0], gla_b_gk[0]], "ag_small")
    tok = start_w_in(1, c_all)
    big = [w_in[0], w_br_hg[0], w_br_gla[0], w_out[0], tr_(w_ff_gate), tr_(w_ff_up), w_ff_down[0]]
    big_bf = [None] + [w.astype(BF16) for w in big[1:]]
    cols = lambda g: jnp.transpose(g, (1, 0, 2)).reshape(g.shape[1], N_DEV * g.shape[2])

    def get_w_in(after):
        for i, handle in enumerate(w_in_pieces):
            land = _split_wait(handle, "ag_w_in_wait%d" % i, after, srcs=w_in_state["src"], lands=w_in_state["land"])
            w_in_state.update(src=handle["srcs"], land=land)
        return _assemble_w_in(_forward_to_sibling(w_in_state["land"][0], "ag_w_in_forward"))

    def get_mix(after):
        g_brh, g_brg, g_out = _split_wait(mix_handle, "ag_mix_wait", after)
        return _gate_cols(cols(g_brh)), _gate_cols(cols(g_brg)), _gate_rows(g_out.reshape(D, D))

    def get_ffn(after):
        g_gate, g_up, g_down = _split_wait(ffn_handle, "ag_ffn_wait", after)
        return (g_gate.reshape(D_FF, D), g_up.reshape(D_FF, D)), g_down.reshape(D_FF, D)

    hg_lb_full = jnp.transpose(lb_g, (1, 2, 0, 3)).reshape(2, 2, HW)
    wgk_k = _layout_wgk(jnp.transpose(wgk_g, (1, 2, 0, 3)).reshape(2, 16, HW)).astype(BF16)
    bgk_k = jnp.transpose(bgk_g, (1, 0, 2)).reshape(1, D)
    onw = jnp.concatenate([jnp.tile(hg_onorm, (1, NH // 2)), jnp.tile(gla_onorm, (1, NH // 2))], axis=1)

    n_mod = w_mod.shape[2]
    a9 = jnp.concatenate([c_ctx[None], c_all[:, 0], jnp.zeros((16 - 1 - N_DEV, D), F32)], axis=0)
    b_loc = lax.dynamic_slice(b_mod, (0, me * n_mod), (1, n_mod))
    s_loc = _mod_fwd(a9, w_mod[0], b_loc)
    s_all, = _all_gather([_tie(s_loc, tok, "tie_s")], "ag_mod")
    mod_all = jnp.transpose(s_all, (1, 0, 2)).reshape(16, N_DEV * n_mod)
    pad8 = lambda m: jnp.concatenate([m.reshape(6, D), jnp.zeros((2, D), F32)], axis=0)
    modc = pad8(mod_all[0])
    modx = pad8(lax.dynamic_slice(mod_all, (1 + me, 0), (1, N_DEV * n_mod))[0])

    gathered = lambda arrs: [(N_DEV,) + a.shape for a in arrs]
    tok = start_w_in(2, s_all)
    tok = start_w_in(3, tok)
    mix_handle, tok = _split_start(_view_whole, gathered(big_bf[1:4]), big_bf[1:4], "ag_mix_start", tok)
    ffn_handle, tok = _split_start(_view_whole, gathered(big_bf[4:]), big_bf[4:], "ag_ffn_start", tok)

    z = (ctx[0], x[0])
    modx = _tie(modx, tok, "tie_mod")
    norms = (norm_pre1, norm_post1, norm_pre2, norm_post2)
    shard = lambda d: jnp.transpose(d.reshape(d.shape[0], N_DEV, -1), (1, 0, 2)).astype(BF16)
    rowshard = lambda d: d.reshape(N_DEV, d.shape[0] // N_DEV, d.shape[1]).astype(BF16)
    sent, w_in_grad = [], {}

    def send_w_in(i, x_after):
        half, rows = W_IN_GRAD_CHUNKS[i]
        handle, tok = _split_start(_view_window(rows), [(N_DEV, rows[1] - rows[0], D)], w_in_grad[half],
                                   "grads_w_in%d_start" % i, x_after)
        w_in_grad[half] = handle["srcs"]
        sent.append(("w_in%d" % i, ["w_in#%d" % i], handle))
        return tok

    def send(names, grads, x_after):
        if names == ("w_in_a",):
            w_in_grad["a"] = list(grads)
            return _tie(x_after, send_w_in(0, x_after), "tie_w_in0")
        if names == ("w_in_b",):
            w_in_grad["b"] = list(grads)
            return x_after
        arrs, leaves = [], []
        for nm, g in zip(names, grads):
            if nm in ("w_gate_t", "w_up_t"):
                arrs.append(rowshard(g))
                leaves.append({"w_gate_t": "w_ff_gate", "w_up_t": "w_ff_up"}[nm])
            elif nm == "w_down":
                arrs.append(rowshard(g))
                leaves.append("w_ff_down")
            elif nm == "w_out":
                arrs.append(rowshard(g[GOFF:GOFF + D]))
                leaves.append(nm)
            else:
                arrs.append(shard(g[:, GOFF:GOFF + D]))
                leaves.append(nm)
        handle, tok = _split_start(_view_block, [a.shape for a in arrs], arrs, "grads_%s_start" % names[0], x_after)
        sent.append((names[0], leaves, handle))
        return _tie(x_after, tok, "tie_" + names[0])

    r = _local_step(z, loss_target[0], modc, modx, norms, onw, hg_lb_full, wgk_k, bgk_k,
                    get_w_in, get_mix, get_ffn, send)
    grad_x = r["grad_x"][None]

    sm_pre, sm_mid, sm_fin = r["sm_pre"], r["sm_mid"], r["sm_final"]
    dmodc = jnp.stack([sm_pre[0], sm_pre[2], sm_mid[4], sm_mid[0], sm_mid[2], sm_fin[0]]).reshape(-1)
    dmodx = jnp.stack([sm_pre[1], sm_pre[3], sm_mid[5], sm_mid[1], sm_mid[3], sm_fin[1]]).reshape(-1)
    on = r["sm_post"][0].reshape(NH, DH)
    pieces = [dmodc, dmodx, sm_pre[4], sm_mid[7], sm_mid[6], sm_fin[2], on[:NH // 2].sum(0), on[NH // 2:].sum(0),
              r["d_lb"][:2].reshape(-1), _unlayout_wgk(r["d_wgk"]).reshape(-1), r["d_bgk"][0]]
    loss_local = (0.5 / D) * jnp.sum(r["loss_vec"])
    pieces.append(jnp.concatenate([loss_local.reshape(1), jnp.zeros((DH - 1,), F32)]))
    sizes = [p.shape[0] for p in pieces]
    pack = jnp.concatenate(pieces).reshape(-1, DH)
    moms = [(m_w_in, v_w_in), (m_w_br_hg, v_w_br_hg), (m_w_br_gla, v_w_br_gla), (m_w_out, v_w_out),
            (m_w_ff_gate, v_w_ff_gate), (m_w_ff_up, v_w_ff_up), (m_w_ff_down, v_w_ff_down)]
    names = ["w_in", "w_br_hg", "w_br_gla", "w_out", "w_ff_gate", "w_ff_up", "w_ff_down"]
    wmv = {nm: (w, m, v) for nm, w, (m, v) in zip(names, big, moms)}
    res = {}

    def update(nm):
        w, m, v = wmv[nm]
        if nm in ("w_ff_gate", "w_ff_up"):
            outs = _adamw(recv[nm], w, tr_(m), tr_(v), "adamw_" + nm)
            res[nm] = [jnp.swapaxes(o, 0, 1)[None] for o in outs]
        else:
            res[nm] = [o[None] for o in _adamw(recv[nm], w, m[0], v[0], "adamw_" + nm)]

    small_handle, tok = _split_start(_view_whole, [(N_DEV,) + pack.shape], [pack], "small_grads_start", pack)
    tok = send_w_in(1, tok)
    recv = {}
    for first, leaves, handle in sent:
        if not first.startswith("w_in"):
            recv.update(zip(leaves, _split_wait(handle, "grads_%s_wait" % first, tok)))
    update("w_ff_gate")
    update("w_ff_up")
    pack_all, = _split_wait(small_handle, "small_grads_wait", [res["w_ff_gate"][0], res["w_ff_up"][0]])
    tot = _sum_devices(pack_all).reshape(-1)
    offs = [sum(sizes[:i]) for i in range(len(sizes))]
    part = lambda i: tot[offs[i]:offs[i] + sizes[i]]
    dmodc_t, dmodx_t = part(0), part(1)
    g_b_mod = (dmodc_t + dmodx_t)[None]
    g_norms = [part(i)[None] for i in (2, 3, 4, 5)]
    g_hg_on, g_gla_on = part(6)[None], part(7)[None]
    lb0 = lax.dynamic_slice(part(8).reshape(2, HW), (0, me * (HW // N_DEV)), (2, HW // N_DEV))
    g_hg_lb = jnp.stack([lb0, -lb0])
    g_wgk = lax.dynamic_slice(part(9).reshape(2, 16, HW), (0, 0, me * (HW // N_DEV)), (2, 16, HW // N_DEV))[None]
    g_bgk = lax.dynamic_slice(part(10).reshape(2, HW), (0, me * (HW // N_DEV)), (2, HW // N_DEV))[None]
    loss = part(11)[0]

    dmx_all = pack_all.reshape(N_DEV, -1)[:, sizes[0]:sizes[0] + sizes[1]]
    d9 = jnp.concatenate([lax.dynamic_slice(dmodc_t[None], (0, me * n_mod), (1, n_mod)),
                          lax.dynamic_slice(dmx_all, (0, me * n_mod), (N_DEV, n_mod)),
                          jnp.zeros((16 - 1 - N_DEV, n_mod), F32)], axis=0)
    g_w_mod, dcc_part = _mod_bwd(a9, d9, w_mod[0])
    cctx_handle, tok = _split_start(_view_whole, [(N_DEV,) + dcc_part.shape], [dcc_part], "c_ctx_start", dcc_part)
    tok = send_w_in(2, tok)
    recv["w_ff_down"] = _tie(recv["w_ff_down"], tok, "tie_down")
    update("w_ff_down")
    res["w_mod"] = [o[None] for o in _adamw(g_w_mod[None], w_mod[0], m_w_mod[0], v_w_mod[0], "adamw_w_mod")]
    for nm in ("w_out", "w_br_hg", "w_br_gla"):
        update(nm)
    dcc_all, = _split_wait(cctx_handle, "c_ctx_wait", [res["w_ff_down"][0], res["w_mod"][0]])
    g_c_ctx = _sum_devices(dcc_all)[0]

    small = [("c_ctx", c_ctx, m_c_ctx, v_c_ctx, g_c_ctx), ("b_mod", b_mod, m_b_mod, v_b_mod, g_b_mod),
             ("norm_pre1", norm_pre1, m_norm_pre1, v_norm_pre1, g_norms[0]),
             ("norm_post1", norm_post1, m_norm_post1, v_norm_post1, g_norms[1]),
             ("norm_pre2", norm_pre2, m_norm_pre2, v_norm_pre2, g_norms[2]),
             ("norm_post2", norm_post2, m_norm_post2, v_norm_post2, g_norms[3]),
             ("hg_lb", hg_lb, m_hg_lb, v_hg_lb, g_hg_lb), ("hg_onorm", hg_onorm, m_hg_onorm, v_hg_onorm, g_hg_on),
             ("gla_w_gk", gla_w_gk, m_gla_w_gk, v_gla_w_gk, g_wgk), ("gla_b_gk", gla_b_gk, m_gla_b_gk, v_gla_b_gk, g_bgk),
             ("gla_onorm", gla_onorm, m_gla_onorm, v_gla_onorm, g_gla_on)]
    flat = lambda k: jnp.concatenate([s[k].reshape(-1) for s in small]).reshape(-1, DH)
    outs = _adamw(flat(4)[None], flat(1), flat(2), flat(3), "adamw_small")
    off = 0
    for nm, w, _, _, _ in small:
        res[nm] = [o.reshape(-1)[off:off + w.size].reshape(w.shape) for o in outs]
        off += w.size

    done = [res[nm][0] for nm in names[1:]] + [res["w_mod"][0], outs[0]]
    sums = []
    for i, (first, leaves, handle) in enumerate(s for s in sent if s[0].startswith("w_in")):
        half = W_IN_GRAD_CHUNKS[i][0]
        land, = _split_wait(handle, "grads_%s_wait" % first, done, srcs=w_in_grad[half])
        w_in_grad[half] = handle["srcs"]
        sums.append(_sum_windows(land, "sum_windows%d" % i))
    major = lambda a: jnp.transpose(a, (2, 0, 1))
    outs = _adamw_rows3(jnp.concatenate(sums, axis=1), major(w_in), major(m_w_in), major(v_w_in), "adamw_w_in")
    res["w_in"] = [jnp.transpose(o, (1, 2, 0)) for o in outs]

    order = ["c_ctx", "w_mod", "b_mod", "norm_pre1", "norm_post1", "norm_pre2", "norm_post2", "w_in", "hg_lb",
             "hg_onorm", "gla_w_gk", "gla_b_gk", "gla_onorm", "w_br_hg", "w_br_gla", "w_out", "w_ff_gate", "w_ff_up",
             "w_ff_down"]
    return (loss, grad_x, *[res[n][k] for k in range(4) for n in order])
```

```python
import functools

import jax
import jax.numpy as jnp
from jax import lax
from jax.experimental import pallas as pl
from jax.experimental.pallas import tpu as pltpu

F32 = jnp.float32
BF16 = jnp.bfloat16
HI = lax.Precision.HIGHEST

N_DEV = 8
D = 1024
CTX = 256
HW = 512
DH = 128
NH = 8
D_FF = 2816
EPS = 1e-6
GLA_NORM = 16.0
CHUNK = 64
TR = 256
NCT = CTX // TR
W_IN_COLS = 7168
MAIN0 = 0
LR0 = 4608
GW = 1152
GOFF = 32
GATE_HG0 = LR0
GATE_GLA0 = LR0 + D
LEVELS = (32, 16, 8)
EXP_CLAMP = 80.0
VMEM_LIMIT = 48 * 1024 * 1024

ADAM_LR, ADAM_B1, ADAM_B2, ADAM_EPS, ADAM_WD, ADAM_STEP = 0.001, 0.9, 0.999, 1e-08, 0.01, 10


def _cp(*sem):
    return pltpu.CompilerParams(dimension_semantics=sem, vmem_limit_bytes=VMEM_LIMIT)


def _sig(x):
    return jax.nn.sigmoid(x)


def _silu(x):
    return x * _sig(x)


def _dsilu(x):
    s = _sig(x)
    return s * (1.0 + x * (1.0 - s))


def _rstd(x):
    return lax.rsqrt(jnp.mean(x * x, axis=-1, keepdims=True) + EPS)


def _rms_bwd(a, y, r):
    return r * (a - y * (r * r) * jnp.mean(a * y, axis=-1, keepdims=True))


def _colsum(x):
    return jnp.sum(x, axis=0, keepdims=True)


def _dot(a, b, dims, precision=None):
    return lax.dot_general(a, b, (dims, ((), ())), preferred_element_type=F32, precision=precision)


NN = ((1,), (0,))
NT = ((1,), (1,))
TN = ((0,), (0,))

SCAN_HEADS_FWD = 4
SCAN_HEADS_BWD = 4


def _split_dot(m, x):
    mb = m.astype(BF16)
    x1 = x.astype(BF16)
    r1 = x - x1.astype(F32)
    x2 = r1.astype(BF16)
    x3 = (r1 - x2.astype(F32)).astype(BF16)
    return _dot(mb, x1, NN) + _dot(mb, x2, NN) + _dot(mb, x3, NN)


def _matmul(a, b, dims, out_dtype, name, tm, tn, tk, a_off=0, m_out=None):
    a_pair = isinstance(a, (tuple, list))
    as_ = list(a) if a_pair else [a]
    a = as_[0]
    pair = isinstance(b, (tuple, list))
    bs = list(b) if pair else [b]
    b1 = bs[0]
    rows = b1.shape[0] * len(bs)
    half = None
    if dims == NN:
        m, k, n = a.shape[0], rows, b1.shape[1]
        a_spec = pl.BlockSpec((tm, tk), lambda i, j, kk: (i, kk + a_off))
        half = b1.shape[0] // tk
        if a_pair:
            assert pair and a.shape[1] == b1.shape[0] and a_off == 0
            a_spec = [pl.BlockSpec((tm, tk), lambda i, j, kk: (i, jnp.minimum(kk, half - 1))),
                      pl.BlockSpec((tm, tk), lambda i, j, kk: (i, jnp.maximum(kk - half, 0)))]
        b_maps = [lambda i, j, kk: (kk, j)] if not pair else [
            lambda i, j, kk: (jnp.minimum(kk, half - 1), j), lambda i, j, kk: (jnp.maximum(kk - half, 0), j)]
        b_specs = [pl.BlockSpec((tk, tn), f) for f in b_maps]
        axis = 2
    elif dims == NT:
        m, k, n = a.shape[0], b1.shape[1], rows
        a_spec = pl.BlockSpec((tm, tk), lambda i, j, kk: (i, kk + a_off))
        half = b1.shape[0] // tn
        b_maps = [lambda i, j, kk: (j, kk)] if not pair else [
            lambda i, j, kk: (jnp.minimum(j, half - 1), kk), lambda i, j, kk: (jnp.maximum(j - half, 0), kk)]
        b_specs = [pl.BlockSpec((tn, tk), f) for f in b_maps]
        axis = 1
    else:
        assert not pair
        m, k = (a.shape[1] if m_out is None else m_out), a.shape[0]
        n = b1.shape[1]
        a_spec = pl.BlockSpec((tk, tm), lambda i, j, kk: (kk, i + a_off))
        b_specs = [pl.BlockSpec((tk, tn), lambda i, j, kk: (kk, j))]
    assert m % tm == 0 and n % tn == 0 and k % tk == 0, (name, m, n, k, tm, tn, tk)
    nk = k // tk
    nb = len(bs)
    na = len(as_)
    assert na == 1 or dims == NN

    def body(*refs):
        a_refs, refs = refs[:na], refs[na:]
        o_ref = refs[nb]
        if pair:
            bv = jnp.where(pl.program_id(axis) < half, refs[0][...], refs[1][...])
        else:
            bv = refs[0][...]
        av = a_refs[0][...] if na == 1 else jnp.where(pl.program_id(2) < half, a_refs[0][...], a_refs[1][...])
        part = _dot(av, bv, dims)
        if nk == 1:
            o_ref[...] = part.astype(o_ref.dtype)
            return
        acc_ref = refs[nb + 1]
        kk = pl.program_id(2)

        @pl.when(kk == 0)
        def _():
            acc_ref[...] = part

        @pl.when(kk > 0)
        def _():
            acc_ref[...] += part

        @pl.when(kk == nk - 1)
        def _():
            o_ref[...] = acc_ref[...].astype(o_ref.dtype)

    return pl.pallas_call(
        body,
        name=name,
        grid=(m // tm, n // tn, nk),
        in_specs=(a_spec if a_pair else [a_spec]) + b_specs,
        out_specs=pl.BlockSpec((tm, tn), lambda i, j, kk: (i, j)),
        out_shape=jax.ShapeDtypeStruct((m, n), out_dtype),
        scratch_shapes=[] if nk == 1 else [pltpu.VMEM((tm, tn), F32)],
        compiler_params=_cp("parallel", "parallel", "arbitrary"),
    )(*as_, *bs)


def _mm_gu_act(h, w_gate_t, w_up_t, name, tm):
    t = h.shape[0]
    tn = D_FF // 2

    def body(a_ref, bg_ref, bu_ref, u_ref, v_ref, act_ref):
        a = a_ref[...]
        u = _dot(a, bg_ref[...], NT)
        v = _dot(a, bu_ref[...], NT)
        u_ref[...] = u.astype(BF16)
        v_ref[...] = v.astype(BF16)
        act_ref[...] = (_silu(u) * v).astype(BF16)

    wspec = pl.BlockSpec((tn, D), lambda i, j: (j, 0))
    ospec = pl.BlockSpec((tm, tn), lambda i, j: (i, j))
    out = jax.ShapeDtypeStruct((t, D_FF), BF16)
    return pl.pallas_call(
        body, name=name, grid=(t // tm, D_FF // tn),
        in_specs=[pl.BlockSpec((tm, D), lambda i, j: (i, 0)), wspec, wspec],
        out_specs=[ospec] * 3, out_shape=[out] * 3,
        compiler_params=_cp("parallel", "parallel"),
    )(h, w_gate_t, w_up_t)


def _mm_down_dx_act(dy, w_down, u, v, name, tm):
    t = dy.shape[0]
    tn = D_FF // 2

    def body(a_ref, b_ref, u_ref, v_ref, du_ref, dv_ref):
        dact = _dot(a_ref[...], b_ref[...], NT)
        u = u_ref[...].astype(F32)
        du_ref[...] = (dact * v_ref[...].astype(F32) * _dsilu(u)).astype(BF16)
        dv_ref[...] = (dact * _silu(u)).astype(BF16)

    ospec = pl.BlockSpec((tm, tn), lambda i, j: (i, j))
    out = jax.ShapeDtypeStruct((t, D_FF), BF16)
    return pl.pallas_call(
        body, name=name, grid=(t // tm, D_FF // tn),
        in_specs=[pl.BlockSpec((tm, D), lambda i, j: (i, 0)), pl.BlockSpec((tn, D), lambda i, j: (j, 0)), ospec, ospec],
        out_specs=[ospec] * 2, out_shape=[out] * 2,
        compiler_params=_cp("parallel", "parallel"),
    )(dy, w_down, u, v)


def _row(c):
    return pl.BlockSpec((TR, c), lambda i: (i, 0))


def _rowcol(width, cb):
    return pl.BlockSpec((TR, width), lambda i: (i, cb))


def _full(shape):
    return pl.BlockSpec(shape, lambda i: (0,) * len(shape))


def _mod_row(mc_ref, mx_ref, k, is_ctx):
    return jnp.where(is_ctx, mc_ref[k:k + 1, :], mx_ref[k:k + 1, :])


def _z_specs():
    return [pl.BlockSpec((TR, D), lambda i: (jnp.minimum(i, NCT - 1), 0)),
            pl.BlockSpec((TR, D), lambda i: (jnp.maximum(i - NCT, 0), 0))]


def _z_tile(c_ref, x_ref, is_ctx):
    return jnp.where(is_ctx, c_ref[...], x_ref[...])


def _acc_row(ref, k, val):
    ref[k:k + 1, :] += val


def _acc_mod(ref, k, is_ctx, val):
    zero = jnp.zeros_like(val)
    ref[k:k + 1, :] += jnp.where(is_ctx, val, zero)
    ref[k + 1:k + 2, :] += jnp.where(is_ctx, zero, val)


def _prenorm(z, nw, modc, modx, i_shift, i_scale, name):
    t = z[0].shape[0] + z[1].shape[0]

    def body(zc_ref, zx_ref, nw_ref, mc_ref, mx_ref, h_ref):
        is_ctx = pl.program_id(0) < NCT
        x = _z_tile(zc_ref, zx_ref, is_ctx)
        n = x * _rstd(x) * nw_ref[...]
        h = n * (1.0 + _mod_row(mc_ref, mx_ref, i_scale, is_ctx)) + _mod_row(mc_ref, mx_ref, i_shift, is_ctx)
        h_ref[...] = h.astype(BF16)

    return pl.pallas_call(
        body, name=name, grid=(t // TR,),
        in_specs=_z_specs() + [_full((1, D)), _full((8, D)), _full((8, D))],
        out_specs=_row(D),
        out_shape=jax.ShapeDtypeStruct((t, D), BF16),
        compiler_params=_cp("parallel"),
    )(*z, nw, modc, modx)


def _hg_lb(lb_ref, d):
    a0 = lb_ref[0, d:d + 1, :]
    a1 = lb_ref[1, d:d + 1, :]
    mx = jnp.maximum(a0, a1)
    e0 = jnp.exp(a0 - mx)
    e1 = jnp.exp(a1 - mx)
    return e0 / (e0 + e1)


def _log_sigmoid(x):
    return jnp.minimum(x, 0.0) - jnp.log(1.0 + jnp.exp(-jnp.abs(x)))


def _gates_fwd(p, hg_lb, wgk, bgk):
    t = p.shape[0]
    seg = lambda j: _rowcol(HW, MAIN0 // HW + j)

    def body(hq_ref, hi_ref, hf_ref, hb_ref, gq_ref, gk_ref, gv_ref, lr_ref, lb_ref, wgk_ref, bgk_ref,
             q_ref, v_ref, kf_ref, kb_ref, gf_ref, gb_ref):
        q_ref[:, :HW] = _silu(hq_ref[...].astype(F32)).astype(BF16)
        q_ref[:, HW:] = (gq_ref[...].astype(F32) * (DH ** -0.5)).astype(BF16)
        v_ref[:, :HW] = hi_ref[...]
        v_ref[:, HW:] = gv_ref[...]
        xg = _dot(lr_ref[...].astype(BF16), wgk_ref[...], NN) + bgk_ref[...]
        for d, (raw_ref, k_ref, g_ref) in enumerate(((hf_ref, kf_ref, gf_ref), (hb_ref, kb_ref, gb_ref))):
            lbd = _hg_lb(lb_ref, d)
            f = lbd + (1.0 - lbd) * _sig(raw_ref[...].astype(F32))
            k_ref[:, :HW] = (1.0 - f).astype(BF16)
            k_ref[:, HW:] = gk_ref[...]
            g_ref[:, :HW] = jnp.log(f)
            g_ref[:, HW:] = _log_sigmoid(xg[:, d * HW:(d + 1) * HW]) * (1.0 / GLA_NORM)

    out = jax.ShapeDtypeStruct((t, D), F32)
    outb = jax.ShapeDtypeStruct((t, D), BF16)
    return pl.pallas_call(
        body, name="gates_fwd", grid=(t // TR,),
        in_specs=[seg(0), seg(1), seg(2), seg(3), seg(5), seg(6), seg(7), _rowcol(DH, LR0 // DH),
                  _full((2, 2, HW)), _full((DH, D)), _full((1, D))],
        out_specs=[_row(D)] * 6,
        out_shape=[outb] * 4 + [out] * 2,
        compiler_params=_cp("parallel"),
    )(p, p, p, p, p, p, p, p, hg_lb, wgk, bgk)


def _post_fwd(o_fw, o_bw, p, onw):
    t = o_fw.shape[0]

    def body(of_ref, ob_ref, g1_ref, g2_ref, w_ref, y_ref):
        for h in range(NH):
            sl = slice(h * DH, (h + 1) * DH)
            o = of_ref[:, sl] + ob_ref[:, sl]
            g_ref = g1_ref if h < NH // 2 else g2_ref
            gs = slice((h % (NH // 2)) * DH, (h % (NH // 2) + 1) * DH)
            n = o * _rstd(o) * w_ref[:, sl]
            y_ref[:, sl] = (n * _silu(g_ref[:, gs].astype(F32))).astype(BF16)

    return pl.pallas_call(
        body, name="post_fwd", grid=(t // TR,),
        in_specs=[_row(D), _row(D), _rowcol(HW, MAIN0 // HW + 4), _rowcol(HW, MAIN0 // HW + 8), _full((1, D))],
        out_specs=_row(D),
        out_shape=jax.ShapeDtypeStruct((t, D), BF16),
        compiler_params=_cp("parallel"),
    )(o_fw, o_bw, p, p, onw)


def _gate_window_specs(col0):
    return [_rowcol(HW, col0 // HW), _rowcol(HW, col0 // HW + 1), _rowcol(DH, (col0 + 2 * HW) // DH)]


def _gate_window(refs):
    return jnp.concatenate([r[...].astype(F32) for r in refs], axis=1)


def _merge_fwd(p, u1, u2):
    t = p.shape[0]

    def body(a0, a1, a2, b0, b1, b2, u1_ref, u2_ref, m_ref):
        f = lambda r: r[...].astype(F32)
        m_ref[...] = (_sig(_gate_window((a0, a1, a2))) * f(u1_ref)
                      + _sig(_gate_window((b0, b1, b2))) * f(u2_ref)).astype(BF16)

    return pl.pallas_call(
        body, name="merge_fwd", grid=(t // TR,),
        in_specs=_gate_window_specs(GATE_HG0) + _gate_window_specs(GATE_GLA0) + [_row(GW), _row(GW)],
        out_specs=_row(GW),
        out_shape=jax.ShapeDtypeStruct((t, GW), BF16),
        compiler_params=_cp("parallel"),
    )(p, p, p, p, p, p, u1, u2)


def _mid_fwd(z, y1, nw_post, nw_pre, modc, modx):
    t = y1.shape[0]

    def body(zc_ref, zx_ref, y_ref, wpo_ref, wpr_ref, mc_ref, mx_ref, z1_ref, h_ref):
        is_ctx = pl.program_id(0) < NCT
        y = y_ref[...].astype(F32)
        z1 = _z_tile(zc_ref, zx_ref, is_ctx) + _mod_row(mc_ref, mx_ref, 2, is_ctx) * (y * _rstd(y) * wpo_ref[...])
        z1_ref[...] = z1
        n = z1 * _rstd(z1) * wpr_ref[...]
        h = n * (1.0 + _mod_row(mc_ref, mx_ref, 4, is_ctx)) + _mod_row(mc_ref, mx_ref, 3, is_ctx)
        h_ref[...] = h.astype(BF16)

    return pl.pallas_call(
        body, name="mid_fwd", grid=(t // TR,),
        in_specs=_z_specs() + [_row(D), _full((1, D)), _full((1, D)), _full((8, D)), _full((8, D))],
        out_specs=[_row(D), _row(D)],
        out_shape=[jax.ShapeDtypeStruct((t, D), F32), jax.ShapeDtypeStruct((t, D), BF16)],
        compiler_params=_cp("parallel"),
    )(*z, y1, nw_post, nw_pre, modc, modx)


def _final(z1, y2, target, nw, modc, modx):
    t = z1.shape[0]

    def body(z1_ref, y_ref, tg_ref, w_ref, mc_ref, mx_ref, dz_ref, dy_ref, loss_ref, sm_ref):
        i = pl.program_id(0)
        is_ctx = i < NCT

        @pl.when(i == 0)
        def _():
            loss_ref[...] = jnp.zeros_like(loss_ref)
            sm_ref[...] = jnp.zeros_like(sm_ref)

        g = _mod_row(mc_ref, mx_ref, 5, is_ctx)
        y = y_ref[...].astype(F32)
        r = _rstd(y)
        w = w_ref[...]
        yr = y * r
        n = yr * w
        e = z1_ref[...] + g * n - tg_ref[...]
        lat = jnp.where(is_ctx, 0.0, 1.0)
        loss_ref[...] += lat * _colsum(e * e)
        dz = e * (lat / D)
        dz_ref[...] = dz
        _acc_mod(sm_ref, 0, is_ctx, _colsum(dz * n))
        dn = dz * g
        _acc_row(sm_ref, 2, _colsum(dn * yr))
        dy_ref[...] = _rms_bwd(dn * w, y, r).astype(BF16)

    return pl.pallas_call(
        body, name="final", grid=(t // TR,),
        in_specs=[_row(D), _row(D), pl.BlockSpec((TR, D), lambda i: (jnp.maximum(i - NCT, 0), 0)),
                  _full((1, D)), _full((8, D)), _full((8, D))],
        out_specs=[_row(D), _row(D), _full((1, D)), _full((8, D))],
        out_shape=[jax.ShapeDtypeStruct((t, D), F32), jax.ShapeDtypeStruct((t, D), BF16),
                   jax.ShapeDtypeStruct((1, D), F32), jax.ShapeDtypeStruct((8, D), F32)],
        compiler_params=_cp("arbitrary"),
    )(z1, y2, target, nw, modc, modx)


def _mid_bwd(dh2, dz, z1, y1, nw_post, nw_pre, modc, modx):
    t = z1.shape[0]

    def body(dh_ref, dz_ref, z1_ref, y_ref, wpo_ref, wpr_ref, mc_ref, mx_ref, dzo_ref, dy_ref, sm_ref):
        i = pl.program_id(0)
        is_ctx = i < NCT

        @pl.when(i == 0)
        def _():
            sm_ref[...] = jnp.zeros_like(sm_ref)

        dh = dh_ref[...].astype(F32)
        z1 = z1_ref[...]
        r = _rstd(z1)
        zr = z1 * r
        wpr = wpr_ref[...]
        n = zr * wpr
        _acc_mod(sm_ref, 0, is_ctx, _colsum(dh))
        _acc_mod(sm_ref, 2, is_ctx, _colsum(dh * n))
        dn = dh * (1.0 + _mod_row(mc_ref, mx_ref, 4, is_ctx))
        _acc_row(sm_ref, 6, _colsum(dn * zr))
        dz1 = dz_ref[...] + _rms_bwd(dn * wpr, z1, r)
        dzo_ref[...] = dz1
        y = y_ref[...].astype(F32)
        r1 = _rstd(y)
        yr = y * r1
        wpo = wpo_ref[...]
        g = _mod_row(mc_ref, mx_ref, 2, is_ctx)
        _acc_mod(sm_ref, 4, is_ctx, _colsum(dz1 * (yr * wpo)))
        dn1 = dz1 * g
        _acc_row(sm_ref, 7, _colsum(dn1 * yr))
        dy_ref[...] = _rms_bwd(dn1 * wpo, y, r1).astype(BF16)

    return pl.pallas_call(
        body, name="mid_bwd", grid=(t // TR,),
        in_specs=[_row(D)] * 4 + [_full((1, D)), _full((1, D)), _full((8, D)), _full((8, D))],
        out_specs=[_row(D), _row(D), _full((8, D))],
        out_shape=[jax.ShapeDtypeStruct((t, D), F32), jax.ShapeDtypeStruct((t, D), BF16),
                   jax.ShapeDtypeStruct((8, D), F32)],
        compiler_params=_cp("arbitrary"),
    )(dh2, dz, z1, y1, nw_post, nw_pre, modc, modx)


def _pre_bwd(dh1, dz, z, nw, modc, modx):
    t = dh1.shape[0]

    def body(dh_ref, dz_ref, zc_ref, zx_ref, w_ref, mc_ref, mx_ref, dzo_ref, sm_ref):
        i = pl.program_id(0)
        is_ctx = i < NCT

        @pl.when(i == 0)
        def _():
            sm_ref[...] = jnp.zeros_like(sm_ref)

        dh = dh_ref[...].astype(F32)
        x = _z_tile(zc_ref, zx_ref, is_ctx)
        r = _rstd(x)
        xr = x * r
        w = w_ref[...]
        _acc_mod(sm_ref, 0, is_ctx, _colsum(dh))
        _acc_mod(sm_ref, 2, is_ctx, _colsum(dh * (xr * w)))
        dn = dh * (1.0 + _mod_row(mc_ref, mx_ref, 1, is_ctx))
        _acc_row(sm_ref, 4, _colsum(dn * xr))
        dzo_ref[...] = dz_ref[...] + _rms_bwd(dn * w, x, r)

    return pl.pallas_call(
        body, name="pre_bwd", grid=(t // TR,),
        in_specs=[_row(D)] * 2 + _z_specs() + [_full((1, D)), _full((8, D)), _full((8, D))],
        out_specs=[pl.BlockSpec((TR, D), lambda i: (jnp.maximum(i - NCT, 0), 0)), _full((8, D))],
        out_shape=[jax.ShapeDtypeStruct((t - CTX, D), F32), jax.ShapeDtypeStruct((8, D), F32)],
        compiler_params=_cp("arbitrary"),
    )(dh1, dz, *z, nw, modc, modx)


def _merge_bwd(dm, p, u1, u2):
    t = dm.shape[0]

    def body(dm_ref, a0, a1, a2, b0, b1, b2, u1_ref, u2_ref, du1_ref, du2_ref, dg_ref):
        dm_ = dm_ref[...].astype(F32)
        s1 = _sig(_gate_window((a0, a1, a2)))
        s2 = _sig(_gate_window((b0, b1, b2)))
        du1_ref[...] = (dm_ * s1).astype(BF16)
        du2_ref[...] = (dm_ * s2).astype(BF16)
        dg_ref[:, :GW] = (dm_ * u1_ref[...].astype(F32) * s1 * (1.0 - s1)).astype(BF16)
        dg_ref[:, GW:] = (dm_ * u2_ref[...].astype(F32) * s2 * (1.0 - s2)).astype(BF16)

    return pl.pallas_call(
        body, name="merge_bwd", grid=(t // TR,),
        in_specs=[_row(GW)] + _gate_window_specs(GATE_HG0) + _gate_window_specs(GATE_GLA0) + [_row(GW), _row(GW)],
        out_specs=[_row(GW), _row(GW), _row(2 * GW)],
        out_shape=[jax.ShapeDtypeStruct((t, GW), BF16), jax.ShapeDtypeStruct((t, GW), BF16),
                   jax.ShapeDtypeStruct((t, 2 * GW), BF16)],
        compiler_params=_cp("parallel"),
    )(dm, p, p, p, p, p, p, u1, u2)


def _post_bwd(dy_hg, dy_gla, o_fw, o_bw, p, onw):
    t = o_fw.shape[0]

    def body(d1_ref, d2_ref, of_ref, ob_ref, g1_ref, g2_ref, w_ref, do_ref, dg_ref, sm_ref):
        @pl.when(pl.program_id(0) == 0)
        def _():
            sm_ref[...] = jnp.zeros_like(sm_ref)

        for h in range(NH):
            sl = slice(h * DH, (h + 1) * DH)
            gs = slice((h % (NH // 2)) * DH, (h % (NH // 2) + 1) * DH)
            g_ref, d_ref = (g1_ref, d1_ref) if h < NH // 2 else (g2_ref, d2_ref)
            o = of_ref[:, sl] + ob_ref[:, sl]
            r = _rstd(o)
            orr = o * r
            w = w_ref[:, sl]
            gt = g_ref[:, gs].astype(F32)
            dy = d_ref[:, gs].astype(F32)
            dg_ref[:, sl] = (dy * (orr * w) * _dsilu(gt)).astype(BF16)
            dn = dy * _silu(gt)
            sm_ref[0:1, sl] += _colsum(dn * orr)
            do_ref[:, sl] = _rms_bwd(dn * w, o, r)

    return pl.pallas_call(
        body, name="post_bwd", grid=(t // TR,),
        in_specs=[_row(HW), _row(HW), _row(D), _row(D), _rowcol(HW, MAIN0 // HW + 4), _rowcol(HW, MAIN0 // HW + 8),
                  _full((1, D))],
        out_specs=[_row(D), _row(D), _full((8, D))],
        out_shape=[jax.ShapeDtypeStruct((t, D), F32), jax.ShapeDtypeStruct((t, D), BF16),
                   jax.ShapeDtypeStruct((8, D), F32)],
        compiler_params=_cp("arbitrary"),
    )(dy_hg, dy_gla, o_fw, o_bw, p, p, onw)


def _gates_bwd(p, hg_lb, wgk, bgk, dgm, dgo, dq_f, dq_b, dv_f, dv_b, dk_f, dk_b, dg_f, dg_b):
    t = p.shape[0]
    seg = lambda j: _rowcol(HW, MAIN0 // HW + j)

    def body(hq_ref, hf_ref, hb_ref, lr_ref, lb_ref, wgk_ref, bgk_ref, dgm_ref, dgo_ref,
             dqf_ref, dqb_ref, dvf_ref, dvb_ref, dkf_ref, dkb_ref, dgf_ref, dgb_ref,
             dp_ref, dlb_ref, dw_ref, db_ref):
        @pl.when(pl.program_id(0) == 0)
        def _():
            dlb_ref[...] = jnp.zeros_like(dlb_ref)
            dw_ref[...] = jnp.zeros_like(dw_ref)
            db_ref[...] = jnp.zeros_like(db_ref)

        c0 = MAIN0

        def put(j, val):
            dp_ref[:, c0 + j * HW:c0 + (j + 1) * HW] = val.astype(BF16)

        dq = dqf_ref[...].astype(F32) + dqb_ref[...].astype(F32)
        dv = dvf_ref[...].astype(F32) + dvb_ref[...].astype(F32)
        put(0, dq[:, :HW] * _dsilu(hq_ref[...].astype(F32)))
        put(1, dv[:, :HW])
        put(5, dq[:, HW:] * (DH ** -0.5))
        put(7, dv[:, HW:])
        put(6, dkf_ref[:, HW:].astype(F32) + dkb_ref[:, HW:].astype(F32))
        dp_ref[:, c0 + 4 * HW:c0 + 5 * HW] = dgo_ref[:, :HW]
        dp_ref[:, c0 + 8 * HW:c0 + 9 * HW] = dgo_ref[:, HW:]
        lr = lr_ref[...].astype(BF16)
        xg = _dot(lr, wgk_ref[...], NN) + bgk_ref[...]
        dxg = []
        for d, (raw_ref, dk_ref, dg_ref) in enumerate(((hf_ref, dkf_ref, dgf_ref), (hb_ref, dkb_ref, dgb_ref))):
            lbd = _hg_lb(lb_ref, d)
            s = _sig(raw_ref[...].astype(F32))
            f = lbd + (1.0 - lbd) * s
            df = dg_ref[:, :HW] / f - dk_ref[:, :HW].astype(F32)
            put(2 + d, df * (1.0 - lbd) * s * (1.0 - s))
            dlb_ref[d:d + 1, :] += _colsum(df * (1.0 - s)) * (lbd * (1.0 - lbd))
            dxg.append(dg_ref[:, HW:] * (1.0 / GLA_NORM) * _sig(-xg[:, d * HW:(d + 1) * HW]))
        dxg = jnp.concatenate(dxg, axis=1)
        db_ref[0:1, :] += _colsum(dxg)
        dxg_b = dxg.astype(BF16)
        dw_ref[...] += _dot(lr, dxg_b, TN)
        dlr = _dot(dxg_b, wgk_ref[...], NT)
        dp_ref[:, LR0:LR0 + DH] = (dlr + dgm_ref[:, :DH].astype(F32)).astype(BF16)
        dp_ref[:, LR0 + DH:GATE_GLA0] = dgm_ref[:, DH:D]
        dp_ref[:, GATE_GLA0:GATE_GLA0 + DH] = dgm_ref[:, D:GW] + dgm_ref[:, GW:GW + DH]
        dp_ref[:, GATE_GLA0 + DH:GATE_GLA0 + GW] = dgm_ref[:, GW + DH:]
        dp_ref[:, GATE_GLA0 + GW:] = jnp.zeros((TR, W_IN_COLS - GATE_GLA0 - GW), BF16)

    return pl.pallas_call(
        body, name="gates_bwd", grid=(t // TR,),
        in_specs=[seg(0), seg(2), seg(3), _rowcol(DH, LR0 // DH), _full((2, 2, HW)), _full((DH, D)), _full((1, D)),
                  _row(2 * GW), _row(D)] + [_row(D)] * 8,
        out_specs=[_row(W_IN_COLS), _full((8, HW)), _full((DH, D)), _full((8, D))],
        out_shape=[jax.ShapeDtypeStruct((t, W_IN_COLS), BF16), jax.ShapeDtypeStruct((8, HW), F32),
                   jax.ShapeDtypeStruct((DH, D), F32), jax.ShapeDtypeStruct((8, D), F32)],
        compiler_params=_cp("arbitrary"),
    )(p, p, p, p, hg_lb, wgk, bgk, dgm, dgo, dq_f, dq_b, dv_f, dv_b, dk_f, dk_b, dg_f, dg_b)


def _scan_consts(rev):
    r = lax.broadcasted_iota(jnp.int32, (CHUNK, CHUNK), 0)
    u = lax.broadcasted_iota(jnp.int32, (CHUNK, CHUNK), 1)
    rp = lax.broadcasted_iota(jnp.int32, (CHUNK, 1), 0)
    if rev:
        r, u, rp = CHUNK - 1 - r, CHUNK - 1 - u, CHUNK - 1 - rp
    tri = jnp.where(u <= r, 1.0, 0.0).astype(F32)
    tri_t = jnp.where(r <= u, 1.0, 0.0).astype(F32)
    lv = []
    for b in LEVELS:
        sh = b.bit_length() - 1
        pair = ((r >> sh) == (u >> sh) + 1) & (((u >> sh) & 1) == 0)
        pair_t = ((u >> sh) == (r >> sh) + 1) & (((r >> sh) & 1) == 0)
        tside = ((rp >> sh) & 1) == 1
        lv.append((pair, pair_t, tside, jnp.where(tside, 1.0, -1.0).astype(F32)))
    bd = LEVELS[-1].bit_length() - 1
    diag = ((r >> bd) == (u >> bd)) & (u <= r)
    diag_t = ((r >> bd) == (u >> bd)) & (r <= u)
    return tri, tri_t, lv, diag, diag_t


def _row_of(pos, rev):
    return CHUNK - 1 - pos if rev else pos


def _chunk_terms(cum, b_scr, consts, rev):
    _, _, lv, _, _ = consts
    terms = []
    for b, (_, _, _, sgn) in zip(LEVELS, lv):
        pieces = []
        for j in range(CHUNK // (2 * b)):
            row = _row_of(2 * b * j + b - 1, rev)
            pieces.append(jnp.broadcast_to(b_scr[row:row + 1, :], (2 * b, DH)))
        if rev:
            pieces = pieces[::-1]
        bnd = pieces[0] if len(pieces) == 1 else jnp.concatenate(pieces, axis=0)
        terms.append(jnp.exp((cum - bnd) * sgn))
    b = LEVELS[-1]
    pieces = []
    for j in range(CHUNK // b):
        if j == 0:
            pieces.append(jnp.zeros((b, DH), F32))
        else:
            row = _row_of(b * j - 1, rev)
            pieces.append(jnp.broadcast_to(b_scr[row:row + 1, :], (b, DH)))
    if rev:
        pieces = pieces[::-1]
    start = jnp.concatenate(pieces, axis=0)
    wq = jnp.exp(jnp.minimum(cum - start, 0.0))
    wk = jnp.exp(jnp.minimum(start - cum, EXP_CLAMP))
    terms.append((wq, wk))
    return terms


def _run_staged(units):
    live = list(units)
    while live:
        nxt = []
        for u in live:
            try:
                next(u)
                nxt.append(u)
            except StopIteration:
                pass
        live = nxt


SCAN_TB = 256
SCAN_CB = SCAN_TB // CHUNK


def _block_order(i, ntb, rev):
    nctx = CTX // SCAN_TB
    if not rev:
        return i
    return jnp.where(i < nctx, nctx - 1 - i, ntb - 1 - (i - nctx))


def _chunk_in_block(j, rev):
    return SCAN_CB - 1 - j if rev else j


def _scan_fwd(q, k, v, g, rev):
    t = q.shape[0]
    nc = t // CHUNK
    hpb = SCAN_HEADS_FWD

    def body(q_ref, k_ref, v_ref, g_ref, o_ref, st_ref, s_scr, b_scr):
        consts = _scan_consts(rev)
        _, _, lv, diag, _ = consts
        masks = [lvl[0] for lvl in lv] + [diag]

        @pl.when(pl.program_id(1) == 0)
        def _():
            s_scr[...] = jnp.zeros_like(s_scr)

        tri = consts[0]
        state = {hh: s_scr[hh] for hh in range(hpb)}

        def unit(hh, j):
            sl = slice(hh * DH, (hh + 1) * DH)
            c = _chunk_in_block(j, rev)
            rows = slice(c * CHUNK, (c + 1) * CHUNK)
            b_ref = b_scr.at[hh * SCAN_CB + j]
            qc, kc, vc, gc = q_ref[rows, sl], k_ref[rows, sl], v_ref[rows, sl], g_ref[rows, sl]
            cum = _split_dot(tri, gc)
            b_ref[...] = cum
            yield
            terms = _chunk_terms(cum, b_ref, consts, rev)
            qf, kf = qc.astype(F32), kc.astype(F32)
            xs = [(jnp.where(tside, qf, kf) * w).astype(BF16) for w, (_, _, tside, _) in zip(terms[:-1], lv)]
            qd, kd = (qf * terms[-1][0]).astype(BF16), (kf * terms[-1][1]).astype(BF16)
            tot = _colsum(gc)
            qe = (qf * jnp.exp(cum)).astype(BF16)
            ke = (kf * jnp.exp(tot - cum)).astype(BF16)
            vb = vc.astype(BF16)
            yield
            scs = [_dot(x, x, NT) for x in xs] + [_dot(qd, kd, NT)]
            kv = _dot(vb, ke, TN)
            yield
            a = jnp.zeros((CHUNK, CHUNK), F32)
            for sc, m in zip(scs, masks):
                a = a + jnp.where(m, sc, 0.0)
            o_intra = _dot(a.astype(BF16), vb, NN)
            yield
            st = state[hh]
            st_ref[hh, c] = st
            o_ref[rows, sl] = o_intra + _dot(qe, st.astype(BF16), NT)
            state[hh] = st * jnp.exp(tot) + kv
            yield

        _run_staged([unit(hh, j) for hh in range(hpb) for j in range(SCAN_CB)])
        for hh in range(hpb):
            s_scr[hh] = state[hh]

    ntb = t // SCAN_TB
    col = pl.BlockSpec((SCAN_TB, hpb * DH), lambda h, i: (_block_order(i, ntb, rev), h))
    return pl.pallas_call(
        body, name="scan_fwd_" + ("bw" if rev else "fw"), grid=(NH // hpb, ntb),
        in_specs=[col] * 4,
        out_specs=[col, pl.BlockSpec((hpb, SCAN_CB, DH, DH), lambda h, i: (h, _block_order(i, ntb, rev), 0, 0))],
        out_shape=[jax.ShapeDtypeStruct((t, D), F32), jax.ShapeDtypeStruct((NH, nc, DH, DH), F32)],
        scratch_shapes=[pltpu.VMEM((hpb, DH, DH), F32), pltpu.VMEM((hpb * SCAN_CB, CHUNK, DH), F32)],
        compiler_params=_cp("parallel", "arbitrary"),
    )(q, k, v, g)


def _scan_bwd(q, k, v, g, do, states, rev):
    t = q.shape[0]
    nc = t // CHUNK
    hpb = SCAN_HEADS_BWD

    def body(q_ref, k_ref, v_ref, g_ref, do_ref, st_ref, dq_ref, dk_ref, dv_ref, dg_ref, ds_scr, b_scr):
        consts = _scan_consts(rev)
        _, tri_t, lv, diag, diag_t = consts
        masks = [(lvl[0], lvl[1]) for lvl in lv] + [(diag, diag_t)]
        @pl.when(pl.program_id(1) == 0)
        def _():
            ds_scr[...] = jnp.zeros_like(ds_scr)

        tri = consts[0]
        dstate = {hh: ds_scr[hh] for hh in range(hpb)}

        def unit(hh, jj):
            sl = slice(hh * DH, (hh + 1) * DH)
            c = _chunk_in_block(SCAN_CB - 1 - jj, rev)
            rows = slice(c * CHUNK, (c + 1) * CHUNK)
            b_ref = b_scr.at[hh * SCAN_CB + jj]
            qc, kc, vc, gc = q_ref[rows, sl], k_ref[rows, sl], v_ref[rows, sl], g_ref[rows, sl]
            dob = do_ref[rows, sl].astype(BF16)
            vb = vc.astype(BF16)
            cum = _split_dot(tri, gc)
            b_ref[...] = cum
            da = _dot(dob, vb, NT)
            da_t = _dot(vb, dob, NT)
            yield
            terms = _chunk_terms(cum, b_ref, consts, rev)
            qf, kf = qc.astype(F32), kc.astype(F32)
            xs = [(jnp.where(tside, qf, kf) * w).astype(BF16) for w, (_, _, tside, _) in zip(terms[:-1], lv)]
            wqd, wkd = terms[-1]
            qdb, kdb = (qf * wqd).astype(BF16), (kf * wkd).astype(BF16)
            tot = _colsum(gc)
            e_tot = jnp.exp(tot)
            e_b = jnp.exp(cum)
            e_t = jnp.exp(tot - cum)
            qeb = (qf * e_b).astype(BF16)
            keb = (kf * e_t).astype(BF16)
            dsym = [(jnp.where(m, da, 0.0) + jnp.where(m_t, da_t, 0.0)).astype(BF16) for m, m_t in masks[:-1]]
            dad = (jnp.where(diag, da, 0.0).astype(BF16), jnp.where(diag_t, da_t, 0.0).astype(BF16))
            yield
            sym = [_dot(x, x, NT) for x in xs]
            dxs = [_dot(d, x, NN) for d, x in zip(dsym, xs)]
            at_d = _dot(kdb, qdb, NT)
            dqt_d = _dot(dad[0], kdb, NN)
            dkt_d = _dot(dad[1], qdb, NN)
            qd = _dot(dob, qeb, TN)
            yield
            a_t = jnp.where(diag_t, at_d, 0.0)
            dq = dqt_d * wqd
            dk = dkt_d * wkd
            db = dqt_d * qdb.astype(F32) - dkt_d * kdb.astype(F32)
            for s, dx, x, w, (_, m_t, tside, sgn) in zip(sym, dxs, xs, terms[:-1], lv):
                a_t = a_t + jnp.where(m_t, s, 0.0)
                dxw = dx * w
                dq = dq + jnp.where(tside, dxw, 0.0)
                dk = dk + jnp.where(tside, 0.0, dxw)
                db = db + (dx * x.astype(F32)) * sgn
            dv_intra = _dot(a_t.astype(BF16), dob, NN)
            st = st_ref[hh, c]
            stb = st.astype(BF16)
            dqe = _dot(dob, stb, NN)
            yield
            dst = dstate[hh]
            dstb = dst.astype(BF16)
            dstate[hh] = dst * e_tot + qd
            dv_ref[rows, sl] = (dv_intra + _dot(keb, dstb, NT)).astype(BF16)
            dke = _dot(vb, dstb, NN)
            yield
            qe = qeb.astype(F32)
            ke = keb.astype(F32)
            dq_ref[rows, sl] = (dq + dqe * e_b).astype(BF16)
            dk_ref[rows, sl] = (dk + dke * e_t).astype(BF16)
            db = db + dqe * qe - dke * ke
            dtot = _colsum(dstb.astype(F32) * stb.astype(F32)) * e_tot + _colsum(dke * ke)
            dg_ref[rows, sl] = _split_dot(tri_t, db) + dtot
            yield

        _run_staged([unit(hh, jj) for hh in range(hpb) for jj in range(SCAN_CB)])
        for hh in range(hpb):
            ds_scr[hh] = dstate[hh]

    ntb = t // SCAN_TB
    blk = lambda i: _block_order(ntb - 1 - i, ntb, rev)
    col = pl.BlockSpec((SCAN_TB, hpb * DH), lambda h, i: (blk(i), h))
    out = jax.ShapeDtypeStruct((t, D), F32)
    outb = jax.ShapeDtypeStruct((t, D), BF16)
    return pl.pallas_call(
        body, name="scan_bwd_" + ("bw" if rev else "fw"), grid=(NH // hpb, ntb),
        in_specs=[col] * 5 + [pl.BlockSpec((hpb, SCAN_CB, DH, DH), lambda h, i: (h, blk(i), 0, 0))],
        out_specs=[col] * 4,
        out_shape=[outb] * 3 + [out],
        scratch_shapes=[pltpu.VMEM((hpb, DH, DH), F32), pltpu.VMEM((hpb * SCAN_CB, CHUNK, DH), F32)],
        compiler_params=_cp("parallel", "arbitrary"),
    )(q, k, v, g, do, states)


W_IN_GRAD_CHUNKS = (("a", (0, 512)), ("b", (0, 256)), ("b", (256, 512)))
W_IN_REF = 6688


def _layout_w_in(w):
    return jnp.pad(w, ((0, 0), (0, W_IN_COLS - W_IN_REF)))


def _unlayout_w_in(d):
    return d[:, :W_IN_REF]


W_IN_PAD = 896
W_IN_PIECE = 256


def _assemble_w_in(g):
    n, r, wp = g.shape
    tr = 256
    tiles = wp // DH

    def body(g_ref, o_ref):
        lane = lax.broadcasted_iota(jnp.int32, (tr, DH), 1)
        for t in range(W_IN_COLS // DH):
            acc = None
            for j in range(n):
                c = DH * t - W_IN_SHARD * j
                if c <= -DH or c >= W_IN_SHARD:
                    continue
                k, s = divmod(c, DH)
                lo = g_ref[j, :, k * DH:(k + 1) * DH] if 0 <= k < tiles else None
                hi = g_ref[j, :, (k + 1) * DH:(k + 2) * DH] if s and 0 <= k + 1 < tiles else None
                if s:
                    zero = jnp.zeros((tr, DH), g.dtype)
                    lo = zero if lo is None else pltpu.roll(lo, DH - s, 1)
                    hi = zero if hi is None else pltpu.roll(hi, DH - s, 1)
                    part = jnp.where(lane < DH - s, lo, hi)
                else:
                    part = lo
                acc = part if acc is None else acc + part
            o_ref[:, t * DH:(t + 1) * DH] = jnp.zeros((tr, DH), g.dtype) if acc is None else acc

    return pl.pallas_call(
        body, name="assemble_w_in", grid=(r // tr,),
        in_specs=[pl.BlockSpec((n, tr, wp), lambda i: (0, i, 0))],
        out_specs=pl.BlockSpec((tr, W_IN_COLS), lambda i: (i, 0)),
        out_shape=jax.ShapeDtypeStruct((r, W_IN_COLS), g.dtype),
        compiler_params=_cp("parallel"),
    )(g)


def _gate_cols(w):
    return jnp.pad(w, ((0, 0), (GOFF, GW - GOFF - D)))


def _gate_rows(w):
    return jnp.pad(w, ((GOFF, GW - GOFF - D), (0, 0)))


def _layout_wgk(w):
    r = w.shape[1]
    top = jnp.concatenate([w[0], jnp.zeros_like(w[0])], axis=1)
    bot = jnp.concatenate([jnp.zeros_like(w[1]), w[1]], axis=1)
    return jnp.concatenate([top, bot, jnp.zeros((DH - 2 * r, D), w.dtype)], axis=0)


def _unlayout_wgk(d, r=16):
    return jnp.stack([d[:r, :HW], d[r:2 * r, HW:]])


def _local_step(z, target, modc, modx, norms, onw, hg_lb, wgk, bgk, get_w_in, get_mix, get_ffn, send):
    n_pre1, n_post1, n_pre2, n_post2 = norms
    t = z[0].shape[0] + z[1].shape[0]
    tm = 1152 if t % 1152 == 0 else 256
    h1 = _prenorm(z, n_pre1, modc, modx, 0, 1, "prenorm1")
    w_in = get_w_in(h1)
    p = _matmul(h1, w_in, NN, BF16, "mm_in", t, 1024, D)
    q, v, k_f, k_b, g_f, g_b = _gates_fwd(p, hg_lb, wgk, bgk)
    o_f, st_f = _scan_fwd(q, k_f, v, g_f, False)
    o_b, st_b = _scan_fwd(q, k_b, v, g_b, True)
    y = _post_fwd(o_f, o_b, p, onw)
    w_br_hg, w_br_gla, w_out = get_mix(y)
    u1 = _matmul(y, w_br_hg, NN, BF16, "mm_br_hg", tm, GW, HW, a_off=0)
    u2 = _matmul(y, w_br_gla, NN, BF16, "mm_br_gla", tm, GW, HW, a_off=1)
    merged = _merge_fwd(p, u1, u2)
    y1 = _matmul(merged, w_out, NN, BF16, "mm_out", tm, 512, GW)
    z1, h2 = _mid_fwd(z, y1, n_post1, n_pre2, modc, modx)
    w_gu_t, w_down = get_ffn(h2)
    u, v_ff, act = _mm_gu_act(h2, w_gu_t[0], w_gu_t[1], "mm_gu", tm)
    y2 = _matmul(act, w_down, NN, BF16, "mm_down", t, 512, D_FF)
    dz, dy2, loss_vec, sm_final = _final(z1, y2, target, n_post2, modc, modx)
    du, dv_ff = _mm_down_dx_act(dy2, w_down, u, v_ff, "mm_down_dx", tm)
    d_w_down = _matmul(act, dy2, TN, BF16, "mm_down_dw", D_FF // 2, 1024, t)
    dh2 = _matmul((du, dv_ff), w_gu_t, NN, BF16, "mm_gu_dx", tm, 512, D_FF)
    d_w_gate_t = _matmul(du, h2, TN, BF16, "mm_gate_dw", D_FF // 2, 1024, t)
    d_w_up_t = _matmul(dv_ff, h2, TN, BF16, "mm_up_dw", D_FF // 2, 1024, t)
    dh2 = send(("w_down", "w_gate_t", "w_up_t"), (d_w_down, d_w_gate_t, d_w_up_t), dh2)
    dz, dy1, sm_mid = _mid_bwd(dh2, dz, z1, y1, n_post1, n_pre2, modc, modx)
    dmerged = _matmul(dy1, w_out, NT, BF16, "mm_out_dx", tm, GW, D)
    d_w_out = _matmul(merged, dy1, TN, BF16, "mm_out_dw", GW, 512, t)
    du1, du2, dgm = _merge_bwd(dmerged, p, u1, u2)
    dy_hg = _matmul(du1, w_br_hg, NT, BF16, "mm_br_hg_dx", tm, HW, GW)
    dy_gla = _matmul(du2, w_br_gla, NT, BF16, "mm_br_gla_dx", tm, HW, GW)
    d_w_br_hg = _matmul(y, du1, TN, BF16, "mm_br_hg_dw", HW, GW, t, a_off=0, m_out=HW)
    d_w_br_gla = _matmul(y, du2, TN, BF16, "mm_br_gla_dw", HW, GW, t, a_off=1, m_out=HW)
    dy_hg = send(("w_out", "w_br_hg", "w_br_gla"), (d_w_out, d_w_br_hg, d_w_br_gla), dy_hg)
    do, dgo, sm_post = _post_bwd(dy_hg, dy_gla, o_f, o_b, p, onw)
    dq_f, dk_f, dv_f, dg_f = _scan_bwd(q, k_f, v, g_f, do, st_f, False)
    dq_b, dk_b, dv_b, dg_b = _scan_bwd(q, k_b, v, g_b, do, st_b, True)
    dp, d_lb, d_wgk, d_bgk = _gates_bwd(p, hg_lb, wgk, bgk, dgm, dgo, dq_f, dq_b, dv_f, dv_b, dk_f, dk_b, dg_f, dg_b)
    d_w_in_a = _matmul(h1, dp, TN, BF16, "mm_in_dw_a", 512, 1024, t, a_off=0, m_out=D // 2)
    dp = send(("w_in_a",), (d_w_in_a,), dp)
    d_w_in_b = _matmul(h1, dp, TN, BF16, "mm_in_dw_b", 512, 1024, t, a_off=1, m_out=D // 2)
    dp = send(("w_in_b",), (d_w_in_b,), dp)
    dh1 = _matmul(dp, w_in, NT, BF16, "mm_in_dx", tm, 512, W_IN_COLS // 2)
    grad_x, sm_pre = _pre_bwd(dh1, dz, z, n_pre1, modc, modx)
    return dict(loss_vec=loss_vec, grad_x=grad_x, sm_final=sm_final, sm_mid=sm_mid, sm_post=sm_post, sm_pre=sm_pre,
                d_lb=d_lb, d_wgk=d_wgk, d_bgk=d_bgk)


MESH = pl.DeviceIdType.MESH
ANY = pl.BlockSpec(memory_space=pl.ANY)
N_REL = N_DEV - 1


def _place():
    return lax.axis_index("x"), lax.axis_index("y"), lax.axis_index("c")


def _slot(p):
    return 4 * p[0] + 2 * p[1] + p[2]


def _all_gather(arrays, name):
    n = len(arrays)

    def body(*refs):
        ins, outs = refs[:n], refs[n:2 * n]
        send_sems, recv_sems, local_sems = refs[2 * n:]
        x, y, c = _place()
        me, sibling = (x, y, c), (x, y, 1 - c)
        chips = [(1 - x, y), (x, 1 - y), (1 - x, 1 - y)]

        def copy(a, k, block, to, src=None):
            dst = outs[a].at[_slot(block)]
            return pltpu.make_async_remote_copy(
                src_ref=dst if src is None else src, dst_ref=dst,
                send_sem=send_sems.at[N_REL * a + k], recv_sem=recv_sems.at[N_REL * a + k],
                device_id=to, device_id_type=MESH)

        mine = [pltpu.make_async_copy(ins[a], outs[a].at[_slot(me)], local_sems.at[a]) for a in range(n)]
        for cp in mine:
            cp.start()
        first = []
        for a in range(n):
            first.append(copy(a, 0, me, sibling, src=ins[a]))
            first += [copy(a, 1 + j, me, (*chip, c), src=ins[a]) for j, chip in enumerate(chips)]
        for cp in first:
            cp.start()
        passed = []
        for j, chip in enumerate(chips):
            for a in range(n):
                copy(a, 1 + j, (*chip, c), me).wait_recv()
                fwd = copy(a, 4 + j, (*chip, c), sibling)
                fwd.start()
                passed.append(fwd)
        for a in range(n):
            copy(a, 0, sibling, me).wait_recv()
        for j, chip in enumerate(chips):
            for a in range(n):
                copy(a, 4 + j, (*chip, 1 - c), me).wait_recv()
        for cp in first + passed:
            cp.wait_send()
        for cp in mine:
            cp.wait()

    return pl.pallas_call(
        body, name=name,
        in_specs=[ANY] * n, out_specs=[ANY] * n,
        out_shape=[jax.ShapeDtypeStruct((N_DEV,) + a.shape, a.dtype) for a in arrays],
        scratch_shapes=[pltpu.SemaphoreType.DMA((N_REL * n,)), pltpu.SemaphoreType.DMA((N_REL * n,)),
                        pltpu.SemaphoreType.DMA((n,))],
    )(*arrays)


def _exchange(arrays, name):
    n = len(arrays)

    def body(*refs):
        ins, outs = refs[:n], refs[n:2 * n]
        send_sems, recv_sems, local_sems = refs[2 * n:]
        x, y, c = _place()
        me = _slot((x, y, c))
        mine = [pltpu.make_async_copy(ins[a].at[me], outs[a].at[me], local_sems.at[a]) for a in range(n)]
        for cp in mine:
            cp.start()
        copies = []
        for a in range(n):
            for k in range(1, N_DEV):
                flip = lambda v, bit: 1 - v if bit else v
                peer = (flip(x, k & 4), flip(y, k & 2), flip(c, k & 1))
                copies.append(pltpu.make_async_remote_copy(
                    src_ref=ins[a].at[_slot(peer)], dst_ref=outs[a].at[me],
                    send_sem=send_sems.at[N_REL * a + k - 1], recv_sem=recv_sems.at[N_REL * a + k - 1],
                    device_id=peer, device_id_type=MESH))
                copies[-1].start()
        i = 0
        for a in range(n):
            for k in range(1, N_DEV):
                flip = lambda v, bit: 1 - v if bit else v
                peer = (flip(x, k & 4), flip(y, k & 2), flip(c, k & 1))
                pltpu.make_async_remote_copy(
                    src_ref=ins[a].at[_slot(peer)], dst_ref=outs[a].at[_slot(peer)],
                    send_sem=send_sems.at[N_REL * a + k - 1], recv_sem=recv_sems.at[N_REL * a + k - 1],
                    device_id=peer, device_id_type=MESH).wait_recv()
                i += 1
        for cp in copies:
            cp.wait_send()
        for cp in mine:
            cp.wait()

    return pl.pallas_call(
        body, name=name,
        in_specs=[ANY] * n, out_specs=[ANY] * n,
        out_shape=[jax.ShapeDtypeStruct(a.shape, a.dtype) for a in arrays],
        scratch_shapes=[pltpu.SemaphoreType.DMA((N_REL * n,)), pltpu.SemaphoreType.DMA((N_REL * n,)),
                        pltpu.SemaphoreType.DMA((n,))],
    )(*arrays)


HBM = pl.BlockSpec(memory_space=pltpu.HBM)
SEM = pl.BlockSpec(memory_space=pltpu.SEMAPHORE)
EFFECT = pltpu.SideEffectType.DATAFLOW_SIDE_EFFECTING


def _peer_of(x, y, c, k):
    flip = lambda v, bit: 1 - v if bit else v
    return flip(x, k & 4), flip(y, k & 2), flip(c, k & 1)


def _view_whole(src, slot):
    return src


def _view_near(src, slot):
    return src


_view_near.peers = (1, 2, 4, 6)


def _view_near_rows(rows):
    def view(src, slot):
        return src.at[pl.ds(rows[0], rows[1] - rows[0])]
    view.peers = _view_near.peers
    view.land = lambda land, slot: land.at[slot, pl.ds(rows[0], rows[1] - rows[0])]
    return view


def _view_block(src, slot):
    return src.at[slot]


W_IN_SHARD = W_IN_REF // N_DEV


def _view_window(rows):
    def view(src, slot):
        col0 = pl.multiple_of((W_IN_SHARD * slot // DH) * DH, DH)
        return src.at[pl.ds(rows[0], rows[1] - rows[0]), pl.ds(col0, D)]
    return view


def _split_copies(view, srcs, lands, send_sems, recv_sems, local_sems):
    x, y, c = _place()
    me = _slot((x, y, c))
    into = getattr(view, "land", lambda land, slot: land.at[slot])
    local, sends, waits = [], [], []
    for a, (src, land) in enumerate(zip(srcs, lands)):
        local.append(pltpu.make_async_copy(view(src, me), into(land, me), local_sems.at[a]))
        for k in getattr(view, "peers", range(1, N_DEV)):
            peer = _peer_of(x, y, c, k)
            mine = view(src, _slot(peer))
            sems = dict(send_sem=send_sems.at[N_REL * a + k - 1], recv_sem=recv_sems.at[N_REL * a + k - 1],
                        device_id=peer, device_id_type=MESH)
            sends.append(pltpu.make_async_remote_copy(src_ref=mine, dst_ref=into(land, me), **sems))
            waits.append(pltpu.make_async_remote_copy(src_ref=mine, dst_ref=into(land, _slot(peer)), **sems))
    return local, sends, waits


def _split_start(view, land_shapes, srcs, name, after, lands=None):
    n = len(srcs)
    if lands is None:
        lands = [lax.empty(shp, s.dtype) for shp, s in zip(land_shapes, srcs)]

    def body(*refs):
        src_refs, land_refs = refs[:n], refs[n:2 * n]
        send_sems, recv_sems, local_sems = refs[2 * n + 1:2 * n + 4]
        token = refs[-1]
        local, sends, _ = _split_copies(view, src_refs, land_refs, send_sems, recv_sems, local_sems)
        for cp in local + sends:
            cp.start()
        token[...] = jnp.zeros_like(token)

    hbm = lambda a: pltpu.with_memory_space_constraint(a, pltpu.HBM)
    out = pl.pallas_call(
        body, name=name,
        out_shape=(pltpu.SemaphoreType.DMA((N_REL * n,)), pltpu.SemaphoreType.DMA((N_REL * n,)),
                   pltpu.SemaphoreType.DMA((n,)),
                   *[pltpu.HBM(s.shape, s.dtype) for s in srcs], *[pltpu.HBM(l.shape, l.dtype) for l in lands],
                   jax.ShapeDtypeStruct((8, DH), F32)),
        in_specs=[HBM] * (2 * n) + [ANY],
        out_specs=(SEM, SEM, SEM, *([HBM] * (2 * n)), pl.BlockSpec(memory_space=pltpu.VMEM)),
        input_output_aliases={i: 3 + i for i in range(2 * n)},
        compiler_params=pltpu.CompilerParams(has_side_effects=EFFECT),
    )(*[hbm(s) for s in srcs], *[hbm(l) for l in lands], after)
    handle = dict(view=view, n=n, sems=out[:3], srcs=list(out[3:3 + n]), lands=list(out[3 + n:3 + 2 * n]))
    return handle, out[-1]


def _split_wait(handle, name, after, srcs=None, lands=None):
    view, n, sems = handle["view"], handle["n"], handle["sems"]
    srcs = handle["srcs"] if srcs is None else srcs
    lands = handle["lands"] if lands is None else lands
    afters = list(after) if isinstance(after, (list, tuple)) else [after]

    def body(*refs):
        src_refs, land_refs = refs[:n], refs[n:2 * n]
        send_sems, recv_sems, local_sems = refs[2 * n:2 * n + 3]
        local, _, waits = _split_copies(view, src_refs, land_refs, send_sems, recv_sems, local_sems)
        for cp in waits:
            cp.wait_send()
            cp.wait_recv()
        for cp in local:
            cp.wait()

    out = pl.pallas_call(
        body, name=name,
        out_shape=(*[pltpu.HBM(s.shape, s.dtype) for s in srcs], *[pltpu.HBM(l.shape, l.dtype) for l in lands]),
        in_specs=[HBM] * (2 * n) + [SEM, SEM, SEM] + [ANY] * len(afters),
        out_specs=tuple([HBM] * (2 * n)),
        input_output_aliases={i: i for i in range(2 * n)},
        compiler_params=pltpu.CompilerParams(has_side_effects=EFFECT),
    )(*srcs, *lands, *sems, *afters)
    handle["srcs"] = list(out[:n])
    return list(out[n:])


def _tie(x, token, name):
    def body(x_ref, t_ref, o_ref):
        pass

    return pl.pallas_call(
        body, name=name, out_shape=jax.ShapeDtypeStruct(x.shape, x.dtype),
        in_specs=[ANY, ANY], out_specs=ANY, input_output_aliases={0: 0},
    )(x, token)


def _forward_to_sibling(land, name):
    def body(land_ref, out_ref, send_sems, recv_sems):
        x, y, c = _place()
        sibling = (x, y, 1 - c)
        chips = [(1 - x, y), (x, 1 - y), (1 - x, 1 - y)]

        def copy(j, core):
            blk = _slot((*chips[j], core))
            return pltpu.make_async_remote_copy(src_ref=land_ref.at[blk], dst_ref=out_ref.at[blk],
                                                send_sem=send_sems.at[j], recv_sem=recv_sems.at[j],
                                                device_id=sibling, device_id_type=MESH)

        sends = [copy(j, c) for j in range(3)]
        for cp in sends:
            cp.start()
        for j in range(3):
            copy(j, 1 - c).wait_recv()
        for cp in sends:
            cp.wait_send()

    return pl.pallas_call(
        body, name=name, in_specs=[ANY], out_specs=ANY, input_output_aliases={0: 0},
        out_shape=jax.ShapeDtypeStruct(land.shape, land.dtype),
        scratch_shapes=[pltpu.SemaphoreType.DMA((3,)), pltpu.SemaphoreType.DMA((3,))],
    )(land)


def _mod_fwd(a, w, b):
    def body(a_ref, w_ref, b_ref, o_ref):
        o_ref[...] = _dot(_silu(a_ref[...]), w_ref[...], NN, precision=HI) + b_ref[...]

    return pl.pallas_call(
        body, name="mod_fwd", out_shape=jax.ShapeDtypeStruct((a.shape[0], w.shape[1]), F32),
        compiler_params=pltpu.CompilerParams(vmem_limit_bytes=VMEM_LIMIT),
    )(a, w, b)


def _mod_bwd(a, d, w):
    def body(a_ref, d_ref, w_ref, dw_ref, dc_ref):
        av = a_ref[...]
        dv = d_ref[...]
        dw_ref[...] = _dot(_silu(av), dv, TN, precision=HI)
        da = _dot(dv[0:8, :], w_ref[...], NT, precision=HI) * _dsilu(av[0:8, :])
        row = lax.broadcasted_iota(jnp.int32, da.shape, 0)
        dc_ref[...] = jnp.where(row == 0, da, 0.0)

    return pl.pallas_call(
        body, name="mod_bwd",
        out_shape=[jax.ShapeDtypeStruct(w.shape, F32), jax.ShapeDtypeStruct((8, w.shape[0]), F32)],
        compiler_params=pltpu.CompilerParams(vmem_limit_bytes=VMEM_LIMIT),
    )(a, d, w)


def _sum_devices(g):
    def body(g_ref, o_ref):
        acc = g_ref[0]
        for i in range(1, g.shape[0]):
            acc = acc + g_ref[i]
        o_ref[...] = acc

    return pl.pallas_call(body, name="sum_devices_%d" % g.shape[1],
                          out_shape=jax.ShapeDtypeStruct(g.shape[1:], F32))(g)


def _sum_windows(g, name):
    n, r, c = g.shape
    tr = 128

    def body(g_ref, o_ref):
        x, y, cc = _place()
        lane0 = (W_IN_SHARD * _slot((x, y, cc))) % DH
        acc = g_ref[0].astype(F32)
        for i in range(1, n):
            acc = acc + g_ref[i].astype(F32)
        o_ref[...] = pltpu.roll(acc, (c - lane0) % c, 1).T

    return pl.pallas_call(
        body, name=name, grid=(r // tr,),
        in_specs=[pl.BlockSpec((n, tr, c), lambda i: (0, i, 0))],
        out_specs=pl.BlockSpec((c, tr), lambda i: (0, i)),
        out_shape=jax.ShapeDtypeStruct((c, r), F32),
        compiler_params=_cp("parallel"),
    )(g)


def _adam_rows(r, c, n):
    budget = 6 * 1024 * 1024
    best = None
    for tr in range(16, r + 1, 16):
        if r % tr == 0 and tr * c * (2 * n + 28) <= budget:
            best = tr
    return best if best is not None else r


def _adamw(g, w, m, v, name):
    n, r, c = g.shape
    tr = _adam_rows(r, c, n)
    bc1 = 1.0 - ADAM_B1 ** ADAM_STEP
    bc2 = 1.0 - ADAM_B2 ** ADAM_STEP

    def body(g_ref, w_ref, m_ref, v_ref, go_ref, d_ref, mo_ref, vo_ref):
        grad = g_ref[0].astype(F32)
        for i in range(1, n):
            grad = grad + g_ref[i].astype(F32)
        go_ref[...] = grad
        m_new = ADAM_B1 * m_ref[...] + (1.0 - ADAM_B1) * grad
        v_new = ADAM_B2 * v_ref[...] + (1.0 - ADAM_B2) * (grad * grad)
        mo_ref[...] = m_new
        vo_ref[...] = v_new
        d_ref[...] = -ADAM_LR * ((m_new / bc1) / (jnp.sqrt(v_new / bc2) + ADAM_EPS) + ADAM_WD * w_ref[...])

    blk = pl.BlockSpec((tr, c), lambda i: (i, 0))
    out = jax.ShapeDtypeStruct((r, c), F32)
    return pl.pallas_call(
        body, name=name, grid=(r // tr,),
        in_specs=[pl.BlockSpec((n, tr, c), lambda i: (0, i, 0)), blk, blk, blk],
        out_specs=[blk] * 4, out_shape=[out] * 4,
        compiler_params=_cp("parallel"),
    )(g, w, m, v)


ADAM_ROWS3 = 168


def _adam_math(grad, w, m, v):
    bc1 = 1.0 - ADAM_B1 ** ADAM_STEP
    bc2 = 1.0 - ADAM_B2 ** ADAM_STEP
    m_new = ADAM_B1 * m + (1.0 - ADAM_B1) * grad
    v_new = ADAM_B2 * v + (1.0 - ADAM_B2) * (grad * grad)
    delta = -ADAM_LR * ((m_new / bc1) / (jnp.sqrt(v_new / bc2) + ADAM_EPS) + ADAM_WD * w)
    return delta, m_new, v_new


def _adamw_rows3(g, w3, m3, v3, name):
    r, _, c = w3.shape
    n = ADAM_ROWS3
    starts = list(range(0, r - n, n)) + [r - n]

    def body(g_hbm, w_hbm, m_hbm, v_hbm, go_hbm, d_hbm, mo_hbm, vo_hbm, gbuf, ibuf, obuf, in_sems, out_sems):
        def fetch(p):
            r0, slot = starts[p], p % 2
            g0 = (r0 // 8) * 8
            cps = [pltpu.make_async_copy(g_hbm.at[pl.ds(g0, n + 8)], gbuf.at[slot], in_sems.at[slot, 0])]
            cps += [pltpu.make_async_copy(h.at[pl.ds(r0, n), 0], ibuf.at[slot, k], in_sems.at[slot, 1 + k])
                    for k, h in enumerate((w_hbm, m_hbm, v_hbm))]
            for cp in cps:
                cp.start()
            return cps

        pending, outs = fetch(0), []
        for p, r0 in enumerate(starts):
            slot = p % 2
            nxt = fetch(p + 1) if p + 1 < len(starts) else []
            for cp in pending:
                cp.wait()
            grad = gbuf[slot, pl.ds(r0 - (r0 // 8) * 8, n), :]
            delta, m_new, v_new = _adam_math(grad, ibuf[slot, 0], ibuf[slot, 1], ibuf[slot, 2])
            for cp in outs:
                cp.wait()
            for k, val in enumerate((grad, delta, m_new, v_new)):
                obuf[slot, k] = val
            outs = [pltpu.make_async_copy(obuf.at[slot, k], h.at[pl.ds(r0, n), 0], out_sems.at[slot, k])
                    for k, h in enumerate((go_hbm, d_hbm, mo_hbm, vo_hbm))]
            for cp in outs:
                cp.start()
            pending = nxt
        for cp in outs:
            cp.wait()

    out = jax.ShapeDtypeStruct(w3.shape, F32)
    return pl.pallas_call(
        body, name=name, in_specs=[ANY] * 4, out_specs=[ANY] * 4, out_shape=[out] * 4,
        scratch_shapes=[pltpu.VMEM((2, n + 8, c), F32), pltpu.VMEM((2, 3, n, c), F32), pltpu.VMEM((2, 4, n, c), F32),
                        pltpu.SemaphoreType.DMA((2, 4)), pltpu.SemaphoreType.DMA((2, 4))],
        compiler_params=pltpu.CompilerParams(vmem_limit_bytes=VMEM_LIMIT),
    )(g, w3, m3, v3)


def kernel(x, c, ctx, c_ctx, w_mod, b_mod, norm_pre1, norm_post1, norm_pre2, norm_post2, w_in, hg_lb, hg_onorm, gla_w_gk, gla_b_gk, gla_onorm, w_br_hg, w_br_gla, w_out, w_ff_gate, w_ff_up, w_ff_down, loss_target, m_c_ctx, m_w_mod, m_b_mod, m_norm_pre1, m_norm_post1, m_norm_pre2, m_norm_post2, m_w_in, m_hg_lb, m_hg_onorm, m_gla_w_gk, m_gla_b_gk, m_gla_onorm, m_w_br_hg, m_w_br_gla, m_w_out, m_w_ff_gate, m_w_ff_up, m_w_ff_down, v_c_ctx, v_w_mod, v_b_mod, v_norm_pre1, v_norm_post1, v_norm_pre2, v_norm_post2, v_w_in, v_hg_lb, v_hg_onorm, v_gla_w_gk, v_gla_b_gk, v_gla_onorm, v_w_br_hg, v_w_br_gla, v_w_out, v_w_ff_gate, v_w_ff_up, v_w_ff_down):
    xi, yi, ci = lax.axis_index("x"), lax.axis_index("y"), lax.axis_index("c")
    me = 4 * xi + 2 * yi + ci
    t = CTX + x.shape[1]

    w_in_pieces, w_in_state = [], {}

    def start_w_in(i, after):
        rows = (i * W_IN_PIECE, (i + 1) * W_IN_PIECE)
        handle, tok = _split_start(_view_near_rows(rows), None, w_in_state["src"], "ag_w_in_start%d" % i, after,
                                   lands=w_in_state["land"])
        w_in_state.update(src=handle["srcs"], land=handle["lands"])
        w_in_pieces.append(handle)
        return tok

    tr_ = lambda a: jnp.swapaxes(a[0], 0, 1)
    w_in_bf = jnp.pad(w_in[0].astype(BF16), ((0, 0), (0, W_IN_PAD - W_IN_SHARD)))
    w_in_state.update(src=[w_in_bf], land=[lax.empty((N_DEV,) + w_in_bf.shape, BF16)])
    gathered = lambda arrs: [(N_DEV,) + a.shape for a in arrs]
    tok = start_w_in(0, c)
    small_in = [c, hg_lb, gla_w_gk[0], gla_b_gk[0]]
    small_handle, tok = _split_start(_view_whole, gathered(small_in), small_in, "ag_small_start", tok)
    tok = start_w_in(1, tok)
    c_all, lb_g, wgk_g, bgk_g = _split_wait(small_handle, "ag_small_wait", tok)
    big = [w_in[0], w_br_hg[0], w_br_gla[0], w_out[0], tr_(w_ff_gate), tr_(w_ff_up), w_ff_down[0]]
    big_bf = [None] + [w.astype(BF16) for w in big[1:]]
    cols = lambda g: jnp.transpose(g, (1, 0, 2)).reshape(g.shape[1], N_DEV * g.shape[2])

    def get_w_in(after):
        for i, handle in enumerate(w_in_pieces):
            land = _split_wait(handle, "ag_w_in_wait%d" % i, after, srcs=w_in_state["src"], lands=w_in_state["land"])
            w_in_state.update(src=handle["srcs"], land=land)
        return _assemble_w_in(_forward_to_sibling(w_in_state["land"][0], "ag_w_in_forward"))

    def get_mix(after):
        g_brh, g_brg, g_out = _split_wait(mix_handle, "ag_mix_wait", after)
        return _gate_cols(cols(g_brh)), _gate_cols(cols(g_brg)), _gate_rows(g_out.reshape(D, D))

    def get_ffn(after):
        g_gate, g_up, g_down = _split_wait(ffn_handle, "ag_ffn_wait", after)
        return (g_gate.reshape(D_FF, D), g_up.reshape(D_FF, D)), g_down.reshape(D_FF, D)

    hg_lb_full = jnp.transpose(lb_g, (1, 2, 0, 3)).reshape(2, 2, HW)
    wgk_k = _layout_wgk(jnp.transpose(wgk_g, (1, 2, 0, 3)).reshape(2, 16, HW)).astype(BF16)
    bgk_k = jnp.transpose(bgk_g, (1, 0, 2)).reshape(1, D)
    onw = jnp.concatenate([jnp.tile(hg_onorm, (1, NH // 2)), jnp.tile(gla_onorm, (1, NH // 2))], axis=1)

    n_mod = w_mod.shape[2]
    a9 = jnp.concatenate([c_ctx[None], c_all[:, 0], jnp.zeros((16 - 1 - N_DEV, D), F32)], axis=0)
    b_loc = lax.dynamic_slice(b_mod, (0, me * n_mod), (1, n_mod))
    s_loc = _mod_fwd(a9, w_mod[0], b_loc)
    mod_handle, tok = _split_start(_view_whole, gathered([s_loc]), [s_loc], "ag_mod_start", s_loc)
    tok = start_w_in(2, tok)
    tok = start_w_in(3, tok)
    s_all, = _split_wait(mod_handle, "ag_mod_wait", tok)
    mod_all = jnp.transpose(s_all, (1, 0, 2)).reshape(16, N_DEV * n_mod)
    pad8 = lambda m: jnp.concatenate([m.reshape(6, D), jnp.zeros((2, D), F32)], axis=0)
    modc = pad8(mod_all[0])
    modx = pad8(lax.dynamic_slice(mod_all, (1 + me, 0), (1, N_DEV * n_mod))[0])

    mix_handle, tok = _split_start(_view_whole, gathered(big_bf[1:4]), big_bf[1:4], "ag_mix_start", s_all)
    ffn_handle, tok = _split_start(_view_whole, gathered(big_bf[4:]), big_bf[4:], "ag_ffn_start", tok)

    z = (ctx[0], x[0])
    modx = _tie(modx, tok, "tie_mod")
    norms = (norm_pre1, norm_post1, norm_pre2, norm_post2)
    shard = lambda d: jnp.transpose(d.reshape(d.shape[0], N_DEV, -1), (1, 0, 2)).astype(BF16)
    rowshard = lambda d: d.reshape(N_DEV, d.shape[0] // N_DEV, d.shape[1]).astype(BF16)
    sent, w_in_grad = [], {}

    def send_w_in(i, x_after):
        half, rows = W_IN_GRAD_CHUNKS[i]
        handle, tok = _split_start(_view_window(rows), [(N_DEV, rows[1] - rows[0], D)], w_in_grad[half],
                                   "grads_w_in%d_start" % i, x_after)
        w_in_grad[half] = handle["srcs"]
        sent.append(("w_in%d" % i, ["w_in#%d" % i], handle))
        return tok

    def send(names, grads, x_after):
        if names == ("w_in_a",):
            w_in_grad["a"] = list(grads)
            return _tie(x_after, send_w_in(0, x_after), "tie_w_in0")
        if names == ("w_in_b",):
            w_in_grad["b"] = list(grads)
            return x_after
        arrs, leaves = [], []
        for nm, g in zip(names, grads):
            if nm in ("w_gate_t", "w_up_t"):
                arrs.append(rowshard(g))
                leaves.append({"w_gate_t": "w_ff_gate", "w_up_t": "w_ff_up"}[nm])
            elif nm == "w_down":
                arrs.append(rowshard(g))
                leaves.append("w_ff_down")
            elif nm == "w_out":
                arrs.append(rowshard(g[GOFF:GOFF + D]))
                leaves.append(nm)
            else:
                arrs.append(shard(g[:, GOFF:GOFF + D]))
                leaves.append(nm)
        handle, tok = _split_start(_view_block, [a.shape for a in arrs], arrs, "grads_%s_start" % names[0], x_after)
        sent.append((names[0], leaves, handle))
        return _tie(x_after, tok, "tie_" + names[0])

    r = _local_step(z, loss_target[0], modc, modx, norms, onw, hg_lb_full, wgk_k, bgk_k,
                    get_w_in, get_mix, get_ffn, send)
    grad_x = r["grad_x"][None]

    sm_pre, sm_mid, sm_fin = r["sm_pre"], r["sm_mid"], r["sm_final"]
    dmodc = jnp.stack([sm_pre[0], sm_pre[2], sm_mid[4], sm_mid[0], sm_mid[2], sm_fin[0]]).reshape(-1)
    dmodx = jnp.stack([sm_pre[1], sm_pre[3], sm_mid[5], sm_mid[1], sm_mid[3], sm_fin[1]]).reshape(-1)
    on = r["sm_post"][0].reshape(NH, DH)
    pieces = [dmodc, dmodx, sm_pre[4], sm_mid[7], sm_mid[6], sm_fin[2], on[:NH // 2].sum(0), on[NH // 2:].sum(0),
              r["d_lb"][:2].reshape(-1), _unlayout_wgk(r["d_wgk"]).reshape(-1), r["d_bgk"][0]]
    loss_local = (0.5 / D) * jnp.sum(r["loss_vec"])
    pieces.append(jnp.concatenate([loss_local.reshape(1), jnp.zeros((DH - 1,), F32)]))
    sizes = [p.shape[0] for p in pieces]
    pack = jnp.concatenate(pieces).reshape(-1, DH)
    moms = [(m_w_in, v_w_in), (m_w_br_hg, v_w_br_hg), (m_w_br_gla, v_w_br_gla), (m_w_out, v_w_out),
            (m_w_ff_gate, v_w_ff_gate), (m_w_ff_up, v_w_ff_up), (m_w_ff_down, v_w_ff_down)]
    names = ["w_in", "w_br_hg", "w_br_gla", "w_out", "w_ff_gate", "w_ff_up", "w_ff_down"]
    wmv = {nm: (w, m, v) for nm, w, (m, v) in zip(names, big, moms)}
    res = {}

    def update(nm):
        w, m, v = wmv[nm]
        if nm in ("w_ff_gate", "w_ff_up"):
            outs = _adamw(recv[nm], w, tr_(m), tr_(v), "adamw_" + nm)
            res[nm] = [jnp.swapaxes(o, 0, 1)[None] for o in outs]
        else:
            res[nm] = [o[None] for o in _adamw(recv[nm], w, m[0], v[0], "adamw_" + nm)]

    small_handle, tok = _split_start(_view_whole, [(N_DEV,) + pack.shape], [pack], "small_grads_start", pack)
    tok = send_w_in(1, tok)
    recv = {}
    for first, leaves, handle in sent:
        if not first.startswith("w_in"):
            recv.update(zip(leaves, _split_wait(handle, "grads_%s_wait" % first, tok)))
    update("w_ff_gate")
    update("w_ff_up")
    pack_all, = _split_wait(small_handle, "small_grads_wait", [res["w_ff_gate"][0], res["w_ff_up"][0]])
    tot = _sum_devices(pack_all).reshape(-1)
    offs = [sum(sizes[:i]) for i in range(len(sizes))]
    part = lambda i: tot[offs[i]:offs[i] + sizes[i]]
    dmodc_t, dmodx_t = part(0), part(1)
    g_b_mod = (dmodc_t + dmodx_t)[None]
    g_norms = [part(i)[None] for i in (2, 3, 4, 5)]
    g_hg_on, g_gla_on = part(6)[None], part(7)[None]
    lb0 = lax.dynamic_slice(part(8).reshape(2, HW), (0, me * (HW // N_DEV)), (2, HW // N_DEV))
    g_hg_lb = jnp.stack([lb0, -lb0])
    g_wgk = lax.dynamic_slice(part(9).reshape(2, 16, HW), (0, 0, me * (HW // N_DEV)), (2, 16, HW // N_DEV))[None]
    g_bgk = lax.dynamic_slice(part(10).reshape(2, HW), (0, me * (HW // N_DEV)), (2, HW // N_DEV))[None]
    loss = part(11)[0]

    dmx_all = pack_all.reshape(N_DEV, -1)[:, sizes[0]:sizes[0] + sizes[1]]
    d9 = jnp.concatenate([lax.dynamic_slice(dmodc_t[None], (0, me * n_mod), (1, n_mod)),
                          lax.dynamic_slice(dmx_all, (0, me * n_mod), (N_DEV, n_mod)),
                          jnp.zeros((16 - 1 - N_DEV, n_mod), F32)], axis=0)
    g_w_mod, dcc_part = _mod_bwd(a9, d9, w_mod[0])
    cctx_handle, tok = _split_start(_view_whole, [(N_DEV,) + dcc_part.shape], [dcc_part], "c_ctx_start", dcc_part)
    tok = send_w_in(2, tok)
    recv["w_ff_down"] = _tie(recv["w_ff_down"], tok, "tie_down")
    update("w_ff_down")
    res["w_mod"] = [o[None] for o in _adamw(g_w_mod[None], w_mod[0], m_w_mod[0], v_w_mod[0], "adamw_w_mod")]
    for nm in ("w_out", "w_br_hg", "w_br_gla"):
        update(nm)
    dcc_all, = _split_wait(cctx_handle, "c_ctx_wait", [res["w_ff_down"][0], res["w_mod"][0]])
    g_c_ctx = _sum_devices(dcc_all)[0]

    small = [("c_ctx", c_ctx, m_c_ctx, v_c_ctx, g_c_ctx), ("b_mod", b_mod, m_b_mod, v_b_mod, g_b_mod),
             ("norm_pre1", norm_pre1, m_norm_pre1, v_norm_pre1, g_norms[0]),
             ("norm_post1", norm_post1, m_norm_post1, v_norm_post1, g_norms[1]),
             ("norm_pre2", norm_pre2, m_norm_pre2, v_norm_pre2, g_norms[2]),
             ("norm_post2", norm_post2, m_norm_post2, v_norm_post2, g_norms[3]),
             ("hg_lb", hg_lb, m_hg_lb, v_hg_lb, g_hg_lb), ("hg_onorm", hg_onorm, m_hg_onorm, v_hg_onorm, g_hg_on),
             ("gla_w_gk", gla_w_gk, m_gla_w_gk, v_gla_w_gk, g_wgk), ("gla_b_gk", gla_b_gk, m_gla_b_gk, v_gla_b_gk, g_bgk),
             ("gla_onorm", gla_onorm, m_gla_onorm, v_gla_onorm, g_gla_on)]
    flat = lambda k: jnp.concatenate([s[k].reshape(-1) for s in small]).reshape(-1, DH)
    outs = _adamw(flat(4)[None], flat(1), flat(2), flat(3), "adamw_small")
    off = 0
    for nm, w, _, _, _ in small:
        res[nm] = [o.reshape(-1)[off:off + w.size].reshape(w.shape) for o in outs]
        off += w.size

    done = [res[nm][0] for nm in names[1:]] + [res["w_mod"][0], outs[0]]
    sums = []
    for i, (first, leaves, handle) in enumerate(s for s in sent if s[0].startswith("w_in")):
        half = W_IN_GRAD_CHUNKS[i][0]
        land, = _split_wait(handle, "grads_%s_wait" % first, done, srcs=w_in_grad[half])
        w_in_grad[half] = handle["srcs"]
        sums.append(_sum_windows(land, "sum_windows%d" % i))
    major = lambda a: jnp.transpose(a, (2, 0, 1))
    outs = _adamw_rows3(jnp.concatenate(sums, axis=1), major(w_in), major(m_w_in), major(v_w_in), "adamw_w_in")
    res["w_in"] = [jnp.transpose(o, (1, 2, 0)) for o in outs]

    order = ["c_ctx", "w_mod", "b_mod", "norm_pre1", "norm_post1", "norm_pre2", "norm_post2", "w_in", "hg_lb",
             "hg_onorm", "gla_w_gk", "gla_b_gk", "gla_onorm", "w_br_hg", "w_br_gla", "w_out", "w_ff_gate", "w_ff_up",
             "w_ff_down"]
    return (loss, grad_x, *[res[n][k] for k in range(4) for n in order])
```

```python
import functools

import jax
import jax.numpy as jnp
from jax import lax
from jax.experimental import pallas as pl
from jax.experimental.pallas import tpu as pltpu

F32 = jnp.float32
BF16 = jnp.bfloat16
HI = lax.Precision.HIGHEST

N_DEV = 8
D = 1024
CTX = 256
HW = 512
DH = 128
NH = 8
D_FF = 2816
EPS = 1e-6
GLA_NORM = 16.0
CHUNK = 64
TR = 256
NCT = CTX // TR
W_IN_COLS = 7168
MAIN0 = 0
LR0 = 4608
GW = 1152
GOFF = 32
GATE_HG0 = LR0
GATE_GLA0 = LR0 + D
LEVELS = (32, 16, 8)
EXP_CLAMP = 80.0
VMEM_LIMIT = 48 * 1024 * 1024

ADAM_LR, ADAM_B1, ADAM_B2, ADAM_EPS, ADAM_WD, ADAM_STEP = 0.001, 0.9, 0.999, 1e-08, 0.01, 10


def _cp(*sem):
    return pltpu.CompilerParams(dimension_semantics=sem, vmem_limit_bytes=VMEM_LIMIT)


def _sig(x):
    return jax.nn.sigmoid(x)


def _silu(x):
    return x * _sig(x)


def _dsilu(x):
    s = _sig(x)
    return s * (1.0 + x * (1.0 - s))


def _rstd(x):
    return lax.rsqrt(jnp.mean(x * x, axis=-1, keepdims=True) + EPS)


def _rms_bwd(a, y, r):
    return r * (a - y * (r * r) * jnp.mean(a * y, axis=-1, keepdims=True))


def _colsum(x):
    return jnp.sum(x, axis=0, keepdims=True)


def _dot(a, b, dims, precision=None):
    return lax.dot_general(a, b, (dims, ((), ())), preferred_element_type=F32, precision=precision)


NN = ((1,), (0,))
NT = ((1,), (1,))
TN = ((0,), (0,))

SCAN_HEADS_FWD = 4
SCAN_HEADS_BWD = 4


def _split_dot(m, x):
    mb = m.astype(BF16)
    x1 = x.astype(BF16)
    r1 = x - x1.astype(F32)
    x2 = r1.astype(BF16)
    x3 = (r1 - x2.astype(F32)).astype(BF16)
    return _dot(mb, x1, NN) + _dot(mb, x2, NN) + _dot(mb, x3, NN)


def _matmul(a, b, dims, out_dtype, name, tm, tn, tk, a_off=0, m_out=None):
    a_pair = isinstance(a, (tuple, list))
    as_ = list(a) if a_pair else [a]
    a = as_[0]
    pair = isinstance(b, (tuple, list))
    bs = list(b) if pair else [b]
    b1 = bs[0]
    rows = b1.shape[0] * len(bs)
    half = None
    if dims == NN:
        m, k, n = a.shape[0], rows, b1.shape[1]
        a_spec = pl.BlockSpec((tm, tk), lambda i, j, kk: (i, kk + a_off))
        half = b1.shape[0] // tk
        if a_pair:
            assert pair and a.shape[1] == b1.shape[0] and a_off == 0
            a_spec = [pl.BlockSpec((tm, tk), lambda i, j, kk: (i, jnp.minimum(kk, half - 1))),
                      pl.BlockSpec((tm, tk), lambda i, j, kk: (i, jnp.maximum(kk - half, 0)))]
        b_maps = [lambda i, j, kk: (kk, j)] if not pair else [
            lambda i, j, kk: (jnp.minimum(kk, half - 1), j), lambda i, j, kk: (jnp.maximum(kk - half, 0), j)]
        b_specs = [pl.BlockSpec((tk, tn), f) for f in b_maps]
        axis = 2
    elif dims == NT:
        m, k, n = a.shape[0], b1.shape[1], rows
        a_spec = pl.BlockSpec((tm, tk), lambda i, j, kk: (i, kk + a_off))
        half = b1.shape[0] // tn
        b_maps = [lambda i, j, kk: (j, kk)] if not pair else [
            lambda i, j, kk: (jnp.minimum(j, half - 1), kk), lambda i, j, kk: (jnp.maximum(j - half, 0), kk)]
        b_specs = [pl.BlockSpec((tn, tk), f) for f in b_maps]
        axis = 1
    else:
        assert not pair
        m, k = (a.shape[1] if m_out is None else m_out), a.shape[0]
        n = b1.shape[1]
        a_spec = pl.BlockSpec((tk, tm), lambda i, j, kk: (kk, i + a_off))
        b_specs = [pl.BlockSpec((tk, tn), lambda i, j, kk: (kk, j))]
    assert m % tm == 0 and n % tn == 0 and k % tk == 0, (name, m, n, k, tm, tn, tk)
    nk = k // tk
    nb = len(bs)
    na = len(as_)
    assert na == 1 or dims == NN

    def body(*refs):
        a_refs, refs = refs[:na], refs[na:]
        o_ref = refs[nb]
        if pair:
            bv = jnp.where(pl.program_id(axis) < half, refs[0][...], refs[1][...])
        else:
            bv = refs[0][...]
        av = a_refs[0][...] if na == 1 else jnp.where(pl.program_id(2) < half, a_refs[0][...], a_refs[1][...])
        part = _dot(av, bv, dims)
        if nk == 1:
            o_ref[...] = part.astype(o_ref.dtype)
            return
        acc_ref = refs[nb + 1]
        kk = pl.program_id(2)

        @pl.when(kk == 0)
        def _():
            acc_ref[...] = part

        @pl.when(kk > 0)
        def _():
            acc_ref[...] += part

        @pl.when(kk == nk - 1)
        def _():
            o_ref[...] = acc_ref[...].astype(o_ref.dtype)

    return pl.pallas_call(
        body,
        name=name,
        grid=(m // tm, n // tn, nk),
        in_specs=(a_spec if a_pair else [a_spec]) + b_specs,
        out_specs=pl.BlockSpec((tm, tn), lambda i, j, kk: (i, j)),
        out_shape=jax.ShapeDtypeStruct((m, n), out_dtype),
        scratch_shapes=[] if nk == 1 else [pltpu.VMEM((tm, tn), F32)],
        compiler_params=_cp("parallel", "parallel", "arbitrary"),
    )(*as_, *bs)


def _mm_gu_act(h, w_gate_t, w_up_t, name, tm):
    t = h.shape[0]
    tn = D_FF // 2

    def body(a_ref, bg_ref, bu_ref, u_ref, v_ref, act_ref):
        a = a_ref[...]
        u = _dot(a, bg_ref[...], NT)
        v = _dot(a, bu_ref[...], NT)
        u_ref[...] = u.astype(BF16)
        v_ref[...] = v.astype(BF16)
        act_ref[...] = (_silu(u) * v).astype(BF16)

    wspec = pl.BlockSpec((tn, D), lambda i, j: (j, 0))
    ospec = pl.BlockSpec((tm, tn), lambda i, j: (i, j))
    out = jax.ShapeDtypeStruct((t, D_FF), BF16)
    return pl.pallas_call(
        body, name=name, grid=(t // tm, D_FF // tn),
        in_specs=[pl.BlockSpec((tm, D), lambda i, j: (i, 0)), wspec, wspec],
        out_specs=[ospec] * 3, out_shape=[out] * 3,
        compiler_params=_cp("parallel", "parallel"),
    )(h, w_gate_t, w_up_t)


def _mm_down_dx_act(dy, w_down, u, v, name, tm):
    t = dy.shape[0]
    tn = D_FF // 2

    def body(a_ref, b_ref, u_ref, v_ref, du_ref, dv_ref):
        dact = _dot(a_ref[...], b_ref[...], NT)
        u = u_ref[...].astype(F32)
        du_ref[...] = (dact * v_ref[...].astype(F32) * _dsilu(u)).astype(BF16)
        dv_ref[...] = (dact * _silu(u)).astype(BF16)

    ospec = pl.BlockSpec((tm, tn), lambda i, j: (i, j))
    out = jax.ShapeDtypeStruct((t, D_FF), BF16)
    return pl.pallas_call(
        body, name=name, grid=(t // tm, D_FF // tn),
        in_specs=[pl.BlockSpec((tm, D), lambda i, j: (i, 0)), pl.BlockSpec((tn, D), lambda i, j: (j, 0)), ospec, ospec],
        out_specs=[ospec] * 2, out_shape=[out] * 2,
        compiler_params=_cp("parallel", "parallel"),
    )(dy, w_down, u, v)


def _row(c):
    return pl.BlockSpec((TR, c), lambda i: (i, 0))


def _rowcol(width, cb):
    return pl.BlockSpec((TR, width), lambda i: (i, cb))


def _full(shape):
    return pl.BlockSpec(shape, lambda i: (0,) * len(shape))


def _mod_row(mc_ref, mx_ref, k, is_ctx):
    return jnp.where(is_ctx, mc_ref[k:k + 1, :], mx_ref[k:k + 1, :])


def _z_specs():
    return [pl.BlockSpec((TR, D), lambda i: (jnp.minimum(i, NCT - 1), 0)),
            pl.BlockSpec((TR, D), lambda i: (jnp.maximum(i - NCT, 0), 0))]


def _z_tile(c_ref, x_ref, is_ctx):
    return jnp.where(is_ctx, c_ref[...], x_ref[...])


def _acc_row(ref, k, val):
    ref[k:k + 1, :] += val


def _acc_mod(ref, k, is_ctx, val):
    zero = jnp.zeros_like(val)
    ref[k:k + 1, :] += jnp.where(is_ctx, val, zero)
    ref[k + 1:k + 2, :] += jnp.where(is_ctx, zero, val)


def _prenorm(z, nw, modc, modx, i_shift, i_scale, name):
    t = z[0].shape[0] + z[1].shape[0]

    def body(zc_ref, zx_ref, nw_ref, mc_ref, mx_ref, h_ref):
        is_ctx = pl.program_id(0) < NCT
        x = _z_tile(zc_ref, zx_ref, is_ctx)
        n = x * _rstd(x) * nw_ref[...]
        h = n * (1.0 + _mod_row(mc_ref, mx_ref, i_scale, is_ctx)) + _mod_row(mc_ref, mx_ref, i_shift, is_ctx)
        h_ref[...] = h.astype(BF16)

    return pl.pallas_call(
        body, name=name, grid=(t // TR,),
        in_specs=_z_specs() + [_full((1, D)), _full((8, D)), _full((8, D))],
        out_specs=_row(D),
        out_shape=jax.ShapeDtypeStruct((t, D), BF16),
        compiler_params=_cp("parallel"),
    )(*z, nw, modc, modx)


def _hg_lb(lb_ref, d):
    a0 = lb_ref[0, d:d + 1, :]
    a1 = lb_ref[1, d:d + 1, :]
    mx = jnp.maximum(a0, a1)
    e0 = jnp.exp(a0 - mx)
    e1 = jnp.exp(a1 - mx)
    return e0 / (e0 + e1)


def _log_sigmoid(x):
    return jnp.minimum(x, 0.0) - jnp.log(1.0 + jnp.exp(-jnp.abs(x)))


def _gates_fwd(p, hg_lb, wgk, bgk):
    t = p.shape[0]
    seg = lambda j: _rowcol(HW, MAIN0 // HW + j)

    def body(hq_ref, hi_ref, hf_ref, hb_ref, gq_ref, gk_ref, gv_ref, lr_ref, lb_ref, wgk_ref, bgk_ref,
             q_ref, v_ref, kf_ref, kb_ref, gf_ref, gb_ref):
        q_ref[:, :HW] = _silu(hq_ref[...].astype(F32)).astype(BF16)
        q_ref[:, HW:] = (gq_ref[...].astype(F32) * (DH ** -0.5)).astype(BF16)
        v_ref[:, :HW] = hi_ref[...]
        v_ref[:, HW:] = gv_ref[...]
        xg = _dot(lr_ref[...].astype(BF16), wgk_ref[...], NN) + bgk_ref[...]
        for d, (raw_ref, k_ref, g_ref) in enumerate(((hf_ref, kf_ref, gf_ref), (hb_ref, kb_ref, gb_ref))):
            lbd = _hg_lb(lb_ref, d)
            f = lbd + (1.0 - lbd) * _sig(raw_ref[...].astype(F32))
            k_ref[:, :HW] = (1.0 - f).astype(BF16)
            k_ref[:, HW:] = gk_ref[...]
            g_ref[:, :HW] = jnp.log(f)
            g_ref[:, HW:] = _log_sigmoid(xg[:, d * HW:(d + 1) * HW]) * (1.0 / GLA_NORM)

    out = jax.ShapeDtypeStruct((t, D), F32)
    outb = jax.ShapeDtypeStruct((t, D), BF16)
    return pl.pallas_call(
        body, name="gates_fwd", grid=(t // TR,),
        in_specs=[seg(0), seg(1), seg(2), seg(3), seg(5), seg(6), seg(7), _rowcol(DH, LR0 // DH),
                  _full((2, 2, HW)), _full((DH, D)), _full((1, D))],
        out_specs=[_row(D)] * 6,
        out_shape=[outb] * 4 + [out] * 2,
        compiler_params=_cp("parallel"),
    )(p, p, p, p, p, p, p, p, hg_lb, wgk, bgk)


def _post_fwd(o_fw, o_bw, p, onw):
    t = o_fw.shape[0]

    def body(of_ref, ob_ref, g1_ref, g2_ref, w_ref, y_ref):
        for h in range(NH):
            sl = slice(h * DH, (h + 1) * DH)
            o = of_ref[:, sl] + ob_ref[:, sl]
            g_ref = g1_ref if h < NH // 2 else g2_ref
            gs = slice((h % (NH // 2)) * DH, (h % (NH // 2) + 1) * DH)
            n = o * _rstd(o) * w_ref[:, sl]
            y_ref[:, sl] = (n * _silu(g_ref[:, gs].astype(F32))).astype(BF16)

    return pl.pallas_call(
        body, name="post_fwd", grid=(t // TR,),
        in_specs=[_row(D), _row(D), _rowcol(HW, MAIN0 // HW + 4), _rowcol(HW, MAIN0 // HW + 8), _full((1, D))],
        out_specs=_row(D),
        out_shape=jax.ShapeDtypeStruct((t, D), BF16),
        compiler_params=_cp("parallel"),
    )(o_fw, o_bw, p, p, onw)


def _gate_window_specs(col0):
    return [_rowcol(HW, col0 // HW), _rowcol(HW, col0 // HW + 1), _rowcol(DH, (col0 + 2 * HW) // DH)]


def _gate_window(refs):
    return jnp.concatenate([r[...].astype(F32) for r in refs], axis=1)


def _branch_merge(y, w_hg, w_gla, p):
    t = y.shape[0]

    def body(y_ref, wh_ref, wg_ref, a0, a1, a2, b0, b1, b2, u1_ref, u2_ref, m_ref):
        u1 = _dot(y_ref[:, :HW], wh_ref[...], NN)
        u2 = _dot(y_ref[:, HW:], wg_ref[...], NN)
        u1_ref[...] = u1.astype(BF16)
        u2_ref[...] = u2.astype(BF16)
        m_ref[...] = (_sig(_gate_window((a0, a1, a2))) * u1 + _sig(_gate_window((b0, b1, b2))) * u2).astype(BF16)

    out = jax.ShapeDtypeStruct((t, GW), BF16)
    return pl.pallas_call(
        body, name="branch_merge", grid=(t // TR,),
        in_specs=[_row(D), _full((HW, GW)), _full((HW, GW))] + _gate_window_specs(GATE_HG0)
        + _gate_window_specs(GATE_GLA0),
        out_specs=[_row(GW)] * 3, out_shape=[out] * 3,
        compiler_params=_cp("parallel"),
    )(y, w_hg, w_gla, p, p, p, p, p, p)


def _mid_fwd(z, y1, nw_post, nw_pre, modc, modx):
    t = y1.shape[0]

    def body(zc_ref, zx_ref, y_ref, wpo_ref, wpr_ref, mc_ref, mx_ref, z1_ref, h_ref):
        is_ctx = pl.program_id(0) < NCT
        y = y_ref[...].astype(F32)
        z1 = _z_tile(zc_ref, zx_ref, is_ctx) + _mod_row(mc_ref, mx_ref, 2, is_ctx) * (y * _rstd(y) * wpo_ref[...])
        z1_ref[...] = z1
        n = z1 * _rstd(z1) * wpr_ref[...]
        h = n * (1.0 + _mod_row(mc_ref, mx_ref, 4, is_ctx)) + _mod_row(mc_ref, mx_ref, 3, is_ctx)
        h_ref[...] = h.astype(BF16)

    return pl.pallas_call(
        body, name="mid_fwd", grid=(t // TR,),
        in_specs=_z_specs() + [_row(D), _full((1, D)), _full((1, D)), _full((8, D)), _full((8, D))],
        out_specs=[_row(D), _row(D)],
        out_shape=[jax.ShapeDtypeStruct((t, D), F32), jax.ShapeDtypeStruct((t, D), BF16)],
        compiler_params=_cp("parallel"),
    )(*z, y1, nw_post, nw_pre, modc, modx)


def _final(z1, y2, target, nw, modc, modx):
    t = z1.shape[0]

    def body(z1_ref, y_ref, tg_ref, w_ref, mc_ref, mx_ref, dz_ref, dy_ref, loss_ref, sm_ref):
        i = pl.program_id(0)
        is_ctx = i < NCT

        @pl.when(i == 0)
        def _():
            loss_ref[...] = jnp.zeros_like(loss_ref)
            sm_ref[...] = jnp.zeros_like(sm_ref)

        g = _mod_row(mc_ref, mx_ref, 5, is_ctx)
        y = y_ref[...].astype(F32)
        r = _rstd(y)
        w = w_ref[...]
        yr = y * r
        n = yr * w
        e = z1_ref[...] + g * n - tg_ref[...]
        lat = jnp.where(is_ctx, 0.0, 1.0)
        loss_ref[...] += lat * _colsum(e * e)
        dz = e * (lat / D)
        dz_ref[...] = dz
        _acc_mod(sm_ref, 0, is_ctx, _colsum(dz * n))
        dn = dz * g
        _acc_row(sm_ref, 2, _colsum(dn * yr))
        dy_ref[...] = _rms_bwd(dn * w, y, r).astype(BF16)

    return pl.pallas_call(
        body, name="final", grid=(t // TR,),
        in_specs=[_row(D), _row(D), pl.BlockSpec((TR, D), lambda i: (jnp.maximum(i - NCT, 0), 0)),
                  _full((1, D)), _full((8, D)), _full((8, D))],
        out_specs=[_row(D), _row(D), _full((1, D)), _full((8, D))],
        out_shape=[jax.ShapeDtypeStruct((t, D), F32), jax.ShapeDtypeStruct((t, D), BF16),
                   jax.ShapeDtypeStruct((1, D), F32), jax.ShapeDtypeStruct((8, D), F32)],
        compiler_params=_cp("arbitrary"),
    )(z1, y2, target, nw, modc, modx)


def _mid_bwd(dh2, dz, z1, y1, nw_post, nw_pre, modc, modx):
    t = z1.shape[0]

    def body(dh_ref, dz_ref, z1_ref, y_ref, wpo_ref, wpr_ref, mc_ref, mx_ref, dzo_ref, dy_ref, sm_ref):
        i = pl.program_id(0)
        is_ctx = i < NCT

        @pl.when(i == 0)
        def _():
            sm_ref[...] = jnp.zeros_like(sm_ref)

        dh = dh_ref[...].astype(F32)
        z1 = z1_ref[...]
        r = _rstd(z1)
        zr = z1 * r
        wpr = wpr_ref[...]
        n = zr * wpr
        _acc_mod(sm_ref, 0, is_ctx, _colsum(dh))
        _acc_mod(sm_ref, 2, is_ctx, _colsum(dh * n))
        dn = dh * (1.0 + _mod_row(mc_ref, mx_ref, 4, is_ctx))
        _acc_row(sm_ref, 6, _colsum(dn * zr))
        dz1 = dz_ref[...] + _rms_bwd(dn * wpr, z1, r)
        dzo_ref[...] = dz1
        y = y_ref[...].astype(F32)
        r1 = _rstd(y)
        yr = y * r1
        wpo = wpo_ref[...]
        g = _mod_row(mc_ref, mx_ref, 2, is_ctx)
        _acc_mod(sm_ref, 4, is_ctx, _colsum(dz1 * (yr * wpo)))
        dn1 = dz1 * g
        _acc_row(sm_ref, 7, _colsum(dn1 * yr))
        dy_ref[...] = _rms_bwd(dn1 * wpo, y, r1).astype(BF16)

    return pl.pallas_call(
        body, name="mid_bwd", grid=(t // TR,),
        in_specs=[_row(D)] * 4 + [_full((1, D)), _full((1, D)), _full((8, D)), _full((8, D))],
        out_specs=[_row(D), _row(D), _full((8, D))],
        out_shape=[jax.ShapeDtypeStruct((t, D), F32), jax.ShapeDtypeStruct((t, D), BF16),
                   jax.ShapeDtypeStruct((8, D), F32)],
        compiler_params=_cp("arbitrary"),
    )(dh2, dz, z1, y1, nw_post, nw_pre, modc, modx)


def _pre_bwd(dh1, dz, z, nw, modc, modx):
    t = dh1.shape[0]

    def body(dh_ref, dz_ref, zc_ref, zx_ref, w_ref, mc_ref, mx_ref, dzo_ref, sm_ref):
        i = pl.program_id(0)
        is_ctx = i < NCT

        @pl.when(i == 0)
        def _():
            sm_ref[...] = jnp.zeros_like(sm_ref)

        dh = dh_ref[...].astype(F32)
        x = _z_tile(zc_ref, zx_ref, is_ctx)
        r = _rstd(x)
        xr = x * r
        w = w_ref[...]
        _acc_mod(sm_ref, 0, is_ctx, _colsum(dh))
        _acc_mod(sm_ref, 2, is_ctx, _colsum(dh * (xr * w)))
        dn = dh * (1.0 + _mod_row(mc_ref, mx_ref, 1, is_ctx))
        _acc_row(sm_ref, 4, _colsum(dn * xr))
        dzo_ref[...] = dz_ref[...] + _rms_bwd(dn * w, x, r)

    return pl.pallas_call(
        body, name="pre_bwd", grid=(t // TR,),
        in_specs=[_row(D)] * 2 + _z_specs() + [_full((1, D)), _full((8, D)), _full((8, D))],
        out_specs=[pl.BlockSpec((TR, D), lambda i: (jnp.maximum(i - NCT, 0), 0)), _full((8, D))],
        out_shape=[jax.ShapeDtypeStruct((t - CTX, D), F32), jax.ShapeDtypeStruct((8, D), F32)],
        compiler_params=_cp("arbitrary"),
    )(dh1, dz, *z, nw, modc, modx)


def _branch_merge_bwd(dm, p, u1, u2, w_hg, w_gla):
    t = dm.shape[0]

    def body(dm_ref, a0, a1, a2, b0, b1, b2, u1_ref, u2_ref, wh_ref, wg_ref, du1_ref, du2_ref, dg_ref, dyh_ref, dyg_ref):
        dm_ = dm_ref[...].astype(F32)
        s1 = _sig(_gate_window((a0, a1, a2)))
        s2 = _sig(_gate_window((b0, b1, b2)))
        du1 = (dm_ * s1).astype(BF16)
        du2 = (dm_ * s2).astype(BF16)
        du1_ref[...] = du1
        du2_ref[...] = du2
        dg_ref[:, :GW] = (dm_ * u1_ref[...].astype(F32) * s1 * (1.0 - s1)).astype(BF16)
        dg_ref[:, GW:] = (dm_ * u2_ref[...].astype(F32) * s2 * (1.0 - s2)).astype(BF16)
        dyh_ref[...] = _dot(du1, wh_ref[...], NT).astype(BF16)
        dyg_ref[...] = _dot(du2, wg_ref[...], NT).astype(BF16)

    return pl.pallas_call(
        body, name="branch_merge_bwd", grid=(t // TR,),
        in_specs=[_row(GW)] + _gate_window_specs(GATE_HG0) + _gate_window_specs(GATE_GLA0)
        + [_row(GW), _row(GW), _full((HW, GW)), _full((HW, GW))],
        out_specs=[_row(GW), _row(GW), _row(2 * GW), _row(HW), _row(HW)],
        out_shape=[jax.ShapeDtypeStruct((t, GW), BF16), jax.ShapeDtypeStruct((t, GW), BF16),
                   jax.ShapeDtypeStruct((t, 2 * GW), BF16), jax.ShapeDtypeStruct((t, HW), BF16),
                   jax.ShapeDtypeStruct((t, HW), BF16)],
        compiler_params=_cp("parallel"),
    )(dm, p, p, p, p, p, p, u1, u2, w_hg, w_gla)


def _post_bwd(dy_hg, dy_gla, o_fw, o_bw, p, onw):
    t = o_fw.shape[0]

    def body(d1_ref, d2_ref, of_ref, ob_ref, g1_ref, g2_ref, w_ref, do_ref, dg_ref, sm_ref):
        @pl.when(pl.program_id(0) == 0)
        def _():
            sm_ref[...] = jnp.zeros_like(sm_ref)

        for h in range(NH):
            sl = slice(h * DH, (h + 1) * DH)
            gs = slice((h % (NH // 2)) * DH, (h % (NH // 2) + 1) * DH)
            g_ref, d_ref = (g1_ref, d1_ref) if h < NH // 2 else (g2_ref, d2_ref)
            o = of_ref[:, sl] + ob_ref[:, sl]
            r = _rstd(o)
            orr = o * r
            w = w_ref[:, sl]
            gt = g_ref[:, gs].astype(F32)
            dy = d_ref[:, gs].astype(F32)
            dg_ref[:, sl] = (dy * (orr * w) * _dsilu(gt)).astype(BF16)
            dn = dy * _silu(gt)
            sm_ref[0:1, sl] += _colsum(dn * orr)
            do_ref[:, sl] = _rms_bwd(dn * w, o, r)

    return pl.pallas_call(
        body, name="post_bwd", grid=(t // TR,),
        in_specs=[_row(HW), _row(HW), _row(D), _row(D), _rowcol(HW, MAIN0 // HW + 4), _rowcol(HW, MAIN0 // HW + 8),
                  _full((1, D))],
        out_specs=[_row(D), _row(D), _full((8, D))],
        out_shape=[jax.ShapeDtypeStruct((t, D), F32), jax.ShapeDtypeStruct((t, D), BF16),
                   jax.ShapeDtypeStruct((8, D), F32)],
        compiler_params=_cp("arbitrary"),
    )(dy_hg, dy_gla, o_fw, o_bw, p, p, onw)


def _gates_bwd(p, hg_lb, wgk, bgk, dgm, dgo, dq_f, dq_b, dv_f, dv_b, dk_f, dk_b, dg_f, dg_b):
    t = p.shape[0]
    seg = lambda j: _rowcol(HW, MAIN0 // HW + j)

    def body(hq_ref, hf_ref, hb_ref, lr_ref, lb_ref, wgk_ref, bgk_ref, dgm_ref, dgo_ref,
             dqf_ref, dqb_ref, dvf_ref, dvb_ref, dkf_ref, dkb_ref, dgf_ref, dgb_ref,
             dp_ref, dlb_ref, dw_ref, db_ref):
        @pl.when(pl.program_id(0) == 0)
        def _():
            dlb_ref[...] = jnp.zeros_like(dlb_ref)
            dw_ref[...] = jnp.zeros_like(dw_ref)
            db_ref[...] = jnp.zeros_like(db_ref)

        c0 = MAIN0

        def put(j, val):
            dp_ref[:, c0 + j * HW:c0 + (j + 1) * HW] = val.astype(BF16)

        dq = dqf_ref[...].astype(F32) + dqb_ref[...].astype(F32)
        dv = dvf_ref[...].astype(F32) + dvb_ref[...].astype(F32)
        put(0, dq[:, :HW] * _dsilu(hq_ref[...].astype(F32)))
        put(1, dv[:, :HW])
        put(5, dq[:, HW:] * (DH ** -0.5))
        put(7, dv[:, HW:])
        put(6, dkf_ref[:, HW:].astype(F32) + dkb_ref[:, HW:].astype(F32))
        dp_ref[:, c0 + 4 * HW:c0 + 5 * HW] = dgo_ref[:, :HW]
        dp_ref[:, c0 + 8 * HW:c0 + 9 * HW] = dgo_ref[:, HW:]
        lr = lr_ref[...].astype(BF16)
        xg = _dot(lr, wgk_ref[...], NN) + bgk_ref[...]
        dxg = []
        for d, (raw_ref, dk_ref, dg_ref) in enumerate(((hf_ref, dkf_ref, dgf_ref), (hb_ref, dkb_ref, dgb_ref))):
            lbd = _hg_lb(lb_ref, d)
            s = _sig(raw_ref[...].astype(F32))
            f = lbd + (1.0 - lbd) * s
            df = dg_ref[:, :HW] / f - dk_ref[:, :HW].astype(F32)
            put(2 + d, df * (1.0 - lbd) * s * (1.0 - s))
            dlb_ref[d:d + 1, :] += _colsum(df * (1.0 - s)) * (lbd * (1.0 - lbd))
            dxg.append(dg_ref[:, HW:] * (1.0 / GLA_NORM) * _sig(-xg[:, d * HW:(d + 1) * HW]))
        dxg = jnp.concatenate(dxg, axis=1)
        db_ref[0:1, :] += _colsum(dxg)
        dxg_b = dxg.astype(BF16)
        dw_ref[...] += _dot(lr, dxg_b, TN)
        dlr = _dot(dxg_b, wgk_ref[...], NT)
        dp_ref[:, LR0:LR0 + DH] = (dlr + dgm_ref[:, :DH].astype(F32)).astype(BF16)
        dp_ref[:, LR0 + DH:GATE_GLA0] = dgm_ref[:, DH:D]
        dp_ref[:, GATE_GLA0:GATE_GLA0 + DH] = dgm_ref[:, D:GW] + dgm_ref[:, GW:GW + DH]
        dp_ref[:, GATE_GLA0 + DH:GATE_GLA0 + GW] = dgm_ref[:, GW + DH:]
        dp_ref[:, GATE_GLA0 + GW:] = jnp.zeros((TR, W_IN_COLS - GATE_GLA0 - GW), BF16)

    return pl.pallas_call(
        body, name="gates_bwd", grid=(t // TR,),
        in_specs=[seg(0), seg(2), seg(3), _rowcol(DH, LR0 // DH), _full((2, 2, HW)), _full((DH, D)), _full((1, D)),
                  _row(2 * GW), _row(D)] + [_row(D)] * 8,
        out_specs=[_row(W_IN_COLS), _full((8, HW)), _full((DH, D)), _full((8, D))],
        out_shape=[jax.ShapeDtypeStruct((t, W_IN_COLS), BF16), jax.ShapeDtypeStruct((8, HW), F32),
                   jax.ShapeDtypeStruct((DH, D), F32), jax.ShapeDtypeStruct((8, D), F32)],
        compiler_params=_cp("arbitrary"),
    )(p, p, p, p, hg_lb, wgk, bgk, dgm, dgo, dq_f, dq_b, dv_f, dv_b, dk_f, dk_b, dg_f, dg_b)


def _scan_consts(rev):
    r = lax.broadcasted_iota(jnp.int32, (CHUNK, CHUNK), 0)
    u = lax.broadcasted_iota(jnp.int32, (CHUNK, CHUNK), 1)
    rp = lax.broadcasted_iota(jnp.int32, (CHUNK, 1), 0)
    if rev:
        r, u, rp = CHUNK - 1 - r, CHUNK - 1 - u, CHUNK - 1 - rp
    tri = jnp.where(u <= r, 1.0, 0.0).astype(F32)
    tri_t = jnp.where(r <= u, 1.0, 0.0).astype(F32)
    lv = []
    for b in LEVELS:
        sh = b.bit_length() - 1
        pair = ((r >> sh) == (u >> sh) + 1) & (((u >> sh) & 1) == 0)
        pair_t = ((u >> sh) == (r >> sh) + 1) & (((r >> sh) & 1) == 0)
        tside = ((rp >> sh) & 1) == 1
        lv.append((pair, pair_t, tside, jnp.where(tside, 1.0, -1.0).astype(F32)))
    bd = LEVELS[-1].bit_length() - 1
    diag = ((r >> bd) == (u >> bd)) & (u <= r)
    diag_t = ((r >> bd) == (u >> bd)) & (r <= u)
    return tri, tri_t, lv, diag, diag_t


def _row_of(pos, rev):
    return CHUNK - 1 - pos if rev else pos


def _chunk_terms(cum, b_scr, consts, rev):
    _, _, lv, _, _ = consts
    terms = []
    for b, (_, _, _, sgn) in zip(LEVELS, lv):
        pieces = []
        for j in range(CHUNK // (2 * b)):
            row = _row_of(2 * b * j + b - 1, rev)
            pieces.append(jnp.broadcast_to(b_scr[row:row + 1, :], (2 * b, DH)))
        if rev:
            pieces = pieces[::-1]
        bnd = pieces[0] if len(pieces) == 1 else jnp.concatenate(pieces, axis=0)
        terms.append(jnp.exp((cum - bnd) * sgn))
    b = LEVELS[-1]
    pieces = []
    for j in range(CHUNK // b):
        if j == 0:
            pieces.append(jnp.zeros((b, DH), F32))
        else:
            row = _row_of(b * j - 1, rev)
            pieces.append(jnp.broadcast_to(b_scr[row:row + 1, :], (b, DH)))
    if rev:
        pieces = pieces[::-1]
    start = jnp.concatenate(pieces, axis=0)
    wq = jnp.exp(jnp.minimum(cum - start, 0.0))
    wk = jnp.exp(jnp.minimum(start - cum, EXP_CLAMP))
    terms.append((wq, wk))
    return terms


def _run_staged(units):
    live = list(units)
    while live:
        nxt = []
        for u in live:
            try:
                next(u)
                nxt.append(u)
            except StopIteration:
                pass
        live = nxt


SCAN_TB = 256
SCAN_CB = SCAN_TB // CHUNK


def _block_order(i, ntb, rev):
    nctx = CTX // SCAN_TB
    if not rev:
        return i
    return jnp.where(i < nctx, nctx - 1 - i, ntb - 1 - (i - nctx))


def _chunk_in_block(j, rev):
    return SCAN_CB - 1 - j if rev else j


def _scan_fwd(q, k, v, g, rev):
    t = q.shape[0]
    nc = t // CHUNK
    hpb = SCAN_HEADS_FWD

    def body(q_ref, k_ref, v_ref, g_ref, o_ref, st_ref, s_scr, b_scr):
        consts = _scan_consts(rev)
        _, _, lv, diag, _ = consts
        masks = [lvl[0] for lvl in lv] + [diag]

        @pl.when(pl.program_id(1) == 0)
        def _():
            s_scr[...] = jnp.zeros_like(s_scr)

        tri = consts[0]
        state = {hh: s_scr[hh] for hh in range(hpb)}

        def unit(hh, j):
            sl = slice(hh * DH, (hh + 1) * DH)
            c = _chunk_in_block(j, rev)
            rows = slice(c * CHUNK, (c + 1) * CHUNK)
            b_ref = b_scr.at[hh * SCAN_CB + j]
            qc, kc, vc, gc = q_ref[rows, sl], k_ref[rows, sl], v_ref[rows, sl], g_ref[rows, sl]
            cum = _split_dot(tri, gc)
            b_ref[...] = cum
            yield
            terms = _chunk_terms(cum, b_ref, consts, rev)
            qf, kf = qc.astype(F32), kc.astype(F32)
            xs = [(jnp.where(tside, qf, kf) * w).astype(BF16) for w, (_, _, tside, _) in zip(terms[:-1], lv)]
            qd, kd = (qf * terms[-1][0]).astype(BF16), (kf * terms[-1][1]).astype(BF16)
            tot = _colsum(gc)
            qe = (qf * jnp.exp(cum)).astype(BF16)
            ke = (kf * jnp.exp(tot - cum)).astype(BF16)
            vb = vc.astype(BF16)
            yield
            scs = [_dot(x, x, NT) for x in xs] + [_dot(qd, kd, NT)]
            kv = _dot(vb, ke, TN)
            yield
            a = jnp.zeros((CHUNK, CHUNK), F32)
            for sc, m in zip(scs, masks):
                a = a + jnp.where(m, sc, 0.0)
            o_intra = _dot(a.astype(BF16), vb, NN)
            yield
            st = state[hh]
            st_ref[hh, c] = st
            o_ref[rows, sl] = o_intra + _dot(qe, st.astype(BF16), NT)
            state[hh] = st * jnp.exp(tot) + kv
            yield

        _run_staged([unit(hh, j) for hh in range(hpb) for j in range(SCAN_CB)])
        for hh in range(hpb):
            s_scr[hh] = state[hh]

    ntb = t // SCAN_TB
    col = pl.BlockSpec((SCAN_TB, hpb * DH), lambda h, i: (_block_order(i, ntb, rev), h))
    return pl.pallas_call(
        body, name="scan_fwd_" + ("bw" if rev else "fw"), grid=(NH // hpb, ntb),
        in_specs=[col] * 4,
        out_specs=[col, pl.BlockSpec((hpb, SCAN_CB, DH, DH), lambda h, i: (h, _block_order(i, ntb, rev), 0, 0))],
        out_shape=[jax.ShapeDtypeStruct((t, D), F32), jax.ShapeDtypeStruct((NH, nc, DH, DH), F32)],
        scratch_shapes=[pltpu.VMEM((hpb, DH, DH), F32), pltpu.VMEM((hpb * SCAN_CB, CHUNK, DH), F32)],
        compiler_params=_cp("parallel", "arbitrary"),
    )(q, k, v, g)


def _scan_bwd(q, k, v, g, do, states, rev):
    t = q.shape[0]
    nc = t // CHUNK
    hpb = SCAN_HEADS_BWD

    def body(q_ref, k_ref, v_ref, g_ref, do_ref, st_ref, dq_ref, dk_ref, dv_ref, dg_ref, ds_scr, b_scr):
        consts = _scan_consts(rev)
        _, tri_t, lv, diag, diag_t = consts
        masks = [(lvl[0], lvl[1]) for lvl in lv] + [(diag, diag_t)]
        @pl.when(pl.program_id(1) == 0)
        def _():
            ds_scr[...] = jnp.zeros_like(ds_scr)

        tri = consts[0]
        dstate = {hh: ds_scr[hh] for hh in range(hpb)}

        def unit(hh, jj):
            sl = slice(hh * DH, (hh + 1) * DH)
            c = _chunk_in_block(SCAN_CB - 1 - jj, rev)
            rows = slice(c * CHUNK, (c + 1) * CHUNK)
            b_ref = b_scr.at[hh * SCAN_CB + jj]
            qc, kc, vc, gc = q_ref[rows, sl], k_ref[rows, sl], v_ref[rows, sl], g_ref[rows, sl]
            dob = do_ref[rows, sl].astype(BF16)
            vb = vc.astype(BF16)
            cum = _split_dot(tri, gc)
            b_ref[...] = cum
            da = _dot(dob, vb, NT)
            da_t = _dot(vb, dob, NT)
            yield
            terms = _chunk_terms(cum, b_ref, consts, rev)
            qf, kf = qc.astype(F32), kc.astype(F32)
            xs = [(jnp.where(tside, qf, kf) * w).astype(BF16) for w, (_, _, tside, _) in zip(terms[:-1], lv)]
            wqd, wkd = terms[-1]
            qdb, kdb = (qf * wqd).astype(BF16), (kf * wkd).astype(BF16)
            tot = _colsum(gc)
            e_tot = jnp.exp(tot)
            e_b = jnp.exp(cum)
            e_t = jnp.exp(tot - cum)
            qeb = (qf * e_b).astype(BF16)
            keb = (kf * e_t).astype(BF16)
            dsym = [(jnp.where(m, da, 0.0) + jnp.where(m_t, da_t, 0.0)).astype(BF16) for m, m_t in masks[:-1]]
            dad = (jnp.where(diag, da, 0.0).astype(BF16), jnp.where(diag_t, da_t, 0.0).astype(BF16))
            yield
            sym = [_dot(x, x, NT) for x in xs]
            dxs = [_dot(d, x, NN) for d, x in zip(dsym, xs)]
            at_d = _dot(kdb, qdb, NT)
            dqt_d = _dot(dad[0], kdb, NN)
            dkt_d = _dot(dad[1], qdb, NN)
            qd = _dot(dob, qeb, TN)
            yield
            a_t = jnp.where(diag_t, at_d, 0.0)
            dq = dqt_d * wqd
            dk = dkt_d * wkd
            db = dqt_d * qdb.astype(F32) - dkt_d * kdb.astype(F32)
            for s, dx, x, w, (_, m_t, tside, sgn) in zip(sym, dxs, xs, terms[:-1], lv):
                a_t = a_t + jnp.where(m_t, s, 0.0)
                dxw = dx * w
                dq = dq + jnp.where(tside, dxw, 0.0)
                dk = dk + jnp.where(tside, 0.0, dxw)
                db = db + (dx * x.astype(F32)) * sgn
            dv_intra = _dot(a_t.astype(BF16), dob, NN)
            st = st_ref[hh, c]
            stb = st.astype(BF16)
            dqe = _dot(dob, stb, NN)
            yield
            dst = dstate[hh]
            dstb = dst.astype(BF16)
            dstate[hh] = dst * e_tot + qd
            dv_ref[rows, sl] = (dv_intra + _dot(keb, dstb, NT)).astype(BF16)
            dke = _dot(vb, dstb, NN)
            yield
            qe = qeb.astype(F32)
            ke = keb.astype(F32)
            dq_ref[rows, sl] = (dq + dqe * e_b).astype(BF16)
            dk_ref[rows, sl] = (dk + dke * e_t).astype(BF16)
            db = db + dqe * qe - dke * ke
            dtot = _colsum(dstb.astype(F32) * stb.astype(F32)) * e_tot + _colsum(dke * ke)
            dg_ref[rows, sl] = _split_dot(tri_t, db) + dtot
            yield

        _run_staged([unit(hh, jj) for hh in range(hpb) for jj in range(SCAN_CB)])
        for hh in range(hpb):
            ds_scr[hh] = dstate[hh]

    ntb = t // SCAN_TB
    blk = lambda i: _block_order(ntb - 1 - i, ntb, rev)
    col = pl.BlockSpec((SCAN_TB, hpb * DH), lambda h, i: (blk(i), h))
    out = jax.ShapeDtypeStruct((t, D), F32)
    outb = jax.ShapeDtypeStruct((t, D), BF16)
    return pl.pallas_call(
        body, name="scan_bwd_" + ("bw" if rev else "fw"), grid=(NH // hpb, ntb),
        in_specs=[col] * 5 + [pl.BlockSpec((hpb, SCAN_CB, DH, DH), lambda h, i: (h, blk(i), 0, 0))],
        out_specs=[col] * 4,
        out_shape=[outb] * 3 + [out],
        scratch_shapes=[pltpu.VMEM((hpb, DH, DH), F32), pltpu.VMEM((hpb * SCAN_CB, CHUNK, DH), F32)],
        compiler_params=_cp("parallel", "arbitrary"),
    )(q, k, v, g, do, states)


W_IN_GRAD_CHUNKS = (("a", (0, 512)), ("b", (0, 256)), ("b", (256, 512)))
W_IN_REF = 6688


def _layout_w_in(w):
    return jnp.pad(w, ((0, 0), (0, W_IN_COLS - W_IN_REF)))


def _unlayout_w_in(d):
    return d[:, :W_IN_REF]


W_IN_PAD = 896
W_IN_PIECE = 256


def _assemble_w_in(g):
    n, r, wp = g.shape
    tr = 256
    tiles = wp // DH

    def body(g_ref, o_ref):
        lane = lax.broadcasted_iota(jnp.int32, (tr, DH), 1)
        for t in range(W_IN_COLS // DH):
            acc = None
            for j in range(n):
                c = DH * t - W_IN_SHARD * j
                if c <= -DH or c >= W_IN_SHARD:
                    continue
                k, s = divmod(c, DH)
                lo = g_ref[j, :, k * DH:(k + 1) * DH] if 0 <= k < tiles else None
                hi = g_ref[j, :, (k + 1) * DH:(k + 2) * DH] if s and 0 <= k + 1 < tiles else None
                if s:
                    zero = jnp.zeros((tr, DH), g.dtype)
                    lo = zero if lo is None else pltpu.roll(lo, DH - s, 1)
                    hi = zero if hi is None else pltpu.roll(hi, DH - s, 1)
                    part = jnp.where(lane < DH - s, lo, hi)
                else:
                    part = lo
                acc = part if acc is None else acc + part
            o_ref[:, t * DH:(t + 1) * DH] = jnp.zeros((tr, DH), g.dtype) if acc is None else acc

    return pl.pallas_call(
        body, name="assemble_w_in", grid=(r // tr,),
        in_specs=[pl.BlockSpec((n, tr, wp), lambda i: (0, i, 0))],
        out_specs=pl.BlockSpec((tr, W_IN_COLS), lambda i: (i, 0)),
        out_shape=jax.ShapeDtypeStruct((r, W_IN_COLS), g.dtype),
        compiler_params=_cp("parallel"),
    )(g)


def _gate_cols(w):
    return jnp.pad(w, ((0, 0), (GOFF, GW - GOFF - D)))


def _gate_rows(w):
    return jnp.pad(w, ((GOFF, GW - GOFF - D), (0, 0)))


def _layout_wgk(w):
    r = w.shape[1]
    top = jnp.concatenate([w[0], jnp.zeros_like(w[0])], axis=1)
    bot = jnp.concatenate([jnp.zeros_like(w[1]), w[1]], axis=1)
    return jnp.concatenate([top, bot, jnp.zeros((DH - 2 * r, D), w.dtype)], axis=0)


def _unlayout_wgk(d, r=16):
    return jnp.stack([d[:r, :HW], d[r:2 * r, HW:]])


def _local_step(z, target, modc, modx, norms, onw, hg_lb, wgk, bgk, get_w_in, get_mix, get_ffn, send):
    n_pre1, n_post1, n_pre2, n_post2 = norms
    t = z[0].shape[0] + z[1].shape[0]
    tm = 1152 if t % 1152 == 0 else 256
    h1 = _prenorm(z, n_pre1, modc, modx, 0, 1, "prenorm1")
    w_in = get_w_in(h1)
    p = _matmul(h1, w_in, NN, BF16, "mm_in", t, 1024, D)
    q, v, k_f, k_b, g_f, g_b = _gates_fwd(p, hg_lb, wgk, bgk)
    o_f, st_f = _scan_fwd(q, k_f, v, g_f, False)
    o_b, st_b = _scan_fwd(q, k_b, v, g_b, True)
    y = _post_fwd(o_f, o_b, p, onw)
    w_br_hg, w_br_gla, w_out = get_mix(y)
    u1, u2, merged = _branch_merge(y, w_br_hg, w_br_gla, p)
    y1 = _matmul(merged, w_out, NN, BF16, "mm_out", tm, 512, GW)
    z1, h2 = _mid_fwd(z, y1, n_post1, n_pre2, modc, modx)
    w_gu_t, w_down = get_ffn(h2)
    u, v_ff, act = _mm_gu_act(h2, w_gu_t[0], w_gu_t[1], "mm_gu", tm)
    y2 = _matmul(act, w_down, NN, BF16, "mm_down", t, 512, D_FF)
    dz, dy2, loss_vec, sm_final = _final(z1, y2, target, n_post2, modc, modx)
    du, dv_ff = _mm_down_dx_act(dy2, w_down, u, v_ff, "mm_down_dx", tm)
    d_w_down = _matmul(act, dy2, TN, BF16, "mm_down_dw", D_FF // 2, 1024, t)
    dh2 = _matmul((du, dv_ff), w_gu_t, NN, BF16, "mm_gu_dx", tm, 512, D_FF)
    d_w_gate_t = _matmul(du, h2, TN, BF16, "mm_gate_dw", D_FF // 2, 1024, t)
    d_w_up_t = _matmul(dv_ff, h2, TN, BF16, "mm_up_dw", D_FF // 2, 1024, t)
    dh2 = send(("w_down", "w_gate_t", "w_up_t"), (d_w_down, d_w_gate_t, d_w_up_t), dh2)
    dz, dy1, sm_mid = _mid_bwd(dh2, dz, z1, y1, n_post1, n_pre2, modc, modx)
    dmerged = _matmul(dy1, w_out, NT, BF16, "mm_out_dx", tm, GW, D)
    d_w_out = _matmul(merged, dy1, TN, BF16, "mm_out_dw", GW, 512, t)
    du1, du2, dgm, dy_hg, dy_gla = _branch_merge_bwd(dmerged, p, u1, u2, w_br_hg, w_br_gla)
    d_w_br_hg = _matmul(y, du1, TN, BF16, "mm_br_hg_dw", HW, GW, t, a_off=0, m_out=HW)
    d_w_br_gla = _matmul(y, du2, TN, BF16, "mm_br_gla_dw", HW, GW, t, a_off=1, m_out=HW)
    dy_hg = send(("w_out", "w_br_hg", "w_br_gla"), (d_w_out, d_w_br_hg, d_w_br_gla), dy_hg)
    do, dgo, sm_post = _post_bwd(dy_hg, dy_gla, o_f, o_b, p, onw)
    dq_f, dk_f, dv_f, dg_f = _scan_bwd(q, k_f, v, g_f, do, st_f, False)
    dq_b, dk_b, dv_b, dg_b = _scan_bwd(q, k_b, v, g_b, do, st_b, True)
    dp, d_lb, d_wgk, d_bgk = _gates_bwd(p, hg_lb, wgk, bgk, dgm, dgo, dq_f, dq_b, dv_f, dv_b, dk_f, dk_b, dg_f, dg_b)
    d_w_in_a = _matmul(h1, dp, TN, BF16, "mm_in_dw_a", 512, 1024, t, a_off=0, m_out=D // 2)
    dp = send(("w_in_a",), (d_w_in_a,), dp)
    d_w_in_b = _matmul(h1, dp, TN, BF16, "mm_in_dw_b", 512, 1024, t, a_off=1, m_out=D // 2)
    dp = send(("w_in_b",), (d_w_in_b,), dp)
    dh1 = _matmul(dp, w_in, NT, BF16, "mm_in_dx", tm, 512, W_IN_COLS // 2)
    grad_x, sm_pre = _pre_bwd(dh1, dz, z, n_pre1, modc, modx)
    return dict(loss_vec=loss_vec, grad_x=grad_x, sm_final=sm_final, sm_mid=sm_mid, sm_post=sm_post, sm_pre=sm_pre,
                d_lb=d_lb, d_wgk=d_wgk, d_bgk=d_bgk)


MESH = pl.DeviceIdType.MESH
ANY = pl.BlockSpec(memory_space=pl.ANY)
N_REL = N_DEV - 1


def _place():
    return lax.axis_index("x"), lax.axis_index("y"), lax.axis_index("c")


def _slot(p):
    return 4 * p[0] + 2 * p[1] + p[2]


def _all_gather(arrays, name):
    n = len(arrays)

    def body(*refs):
        ins, outs = refs[:n], refs[n:2 * n]
        send_sems, recv_sems, local_sems = refs[2 * n:]
        x, y, c = _place()
        me, sibling = (x, y, c), (x, y, 1 - c)
        chips = [(1 - x, y), (x, 1 - y), (1 - x, 1 - y)]

        def copy(a, k, block, to, src=None):
            dst = outs[a].at[_slot(block)]
            return pltpu.make_async_remote_copy(
                src_ref=dst if src is None else src, dst_ref=dst,
                send_sem=send_sems.at[N_REL * a + k], recv_sem=recv_sems.at[N_REL * a + k],
                device_id=to, device_id_type=MESH)

        mine = [pltpu.make_async_copy(ins[a], outs[a].at[_slot(me)], local_sems.at[a]) for a in range(n)]
        for cp in mine:
            cp.start()
        first = []
        for a in range(n):
            first.append(copy(a, 0, me, sibling, src=ins[a]))
            first += [copy(a, 1 + j, me, (*chip, c), src=ins[a]) for j, chip in enumerate(chips)]
        for cp in first:
            cp.start()
        passed = []
        for j, chip in enumerate(chips):
            for a in range(n):
                copy(a, 1 + j, (*chip, c), me).wait_recv()
                fwd = copy(a, 4 + j, (*chip, c), sibling)
                fwd.start()
                passed.append(fwd)
        for a in range(n):
            copy(a, 0, sibling, me).wait_recv()
        for j, chip in enumerate(chips):
            for a in range(n):
                copy(a, 4 + j, (*chip, 1 - c), me).wait_recv()
        for cp in first + passed:
            cp.wait_send()
        for cp in mine:
            cp.wait()

    return pl.pallas_call(
        body, name=name,
        in_specs=[ANY] * n, out_specs=[ANY] * n,
        out_shape=[jax.ShapeDtypeStruct((N_DEV,) + a.shape, a.dtype) for a in arrays],
        scratch_shapes=[pltpu.SemaphoreType.DMA((N_REL * n,)), pltpu.SemaphoreType.DMA((N_REL * n,)),
                        pltpu.SemaphoreType.DMA((n,))],
    )(*arrays)


def _exchange(arrays, name):
    n = len(arrays)

    def body(*refs):
        ins, outs = refs[:n], refs[n:2 * n]
        send_sems, recv_sems, local_sems = refs[2 * n:]
        x, y, c = _place()
        me = _slot((x, y, c))
        mine = [pltpu.make_async_copy(ins[a].at[me], outs[a].at[me], local_sems.at[a]) for a in range(n)]
        for cp in mine:
            cp.start()
        copies = []
        for a in range(n):
            for k in range(1, N_DEV):
                flip = lambda v, bit: 1 - v if bit else v
                peer = (flip(x, k & 4), flip(y, k & 2), flip(c, k & 1))
                copies.append(pltpu.make_async_remote_copy(
                    src_ref=ins[a].at[_slot(peer)], dst_ref=outs[a].at[me],
                    send_sem=send_sems.at[N_REL * a + k - 1], recv_sem=recv_sems.at[N_REL * a + k - 1],
                    device_id=peer, device_id_type=MESH))
                copies[-1].start()
        i = 0
        for a in range(n):
            for k in range(1, N_DEV):
                flip = lambda v, bit: 1 - v if bit else v
                peer = (flip(x, k & 4), flip(y, k & 2), flip(c, k & 1))
                pltpu.make_async_remote_copy(
                    src_ref=ins[a].at[_slot(peer)], dst_ref=outs[a].at[_slot(peer)],
                    send_sem=send_sems.at[N_REL * a + k - 1], recv_sem=recv_sems.at[N_REL * a + k - 1],
                    device_id=peer, device_id_type=MESH).wait_recv()
                i += 1
        for cp in copies:
            cp.wait_send()
        for cp in mine:
            cp.wait()

    return pl.pallas_call(
        body, name=name,
        in_specs=[ANY] * n, out_specs=[ANY] * n,
        out_shape=[jax.ShapeDtypeStruct(a.shape, a.dtype) for a in arrays],
        scratch_shapes=[pltpu.SemaphoreType.DMA((N_REL * n,)), pltpu.SemaphoreType.DMA((N_REL * n,)),
                        pltpu.SemaphoreType.DMA((n,))],
    )(*arrays)


HBM = pl.BlockSpec(memory_space=pltpu.HBM)
SEM = pl.BlockSpec(memory_space=pltpu.SEMAPHORE)
EFFECT = pltpu.SideEffectType.DATAFLOW_SIDE_EFFECTING


def _peer_of(x, y, c, k):
    flip = lambda v, bit: 1 - v if bit else v
    return flip(x, k & 4), flip(y, k & 2), flip(c, k & 1)


def _view_whole(src, slot):
    return src


def _view_near(src, slot):
    return src


_view_near.peers = (1, 2, 4, 6)


def _view_near_rows(rows):
    def view(src, slot):
        return src.at[pl.ds(rows[0], rows[1] - rows[0])]
    view.peers = _view_near.peers
    view.land = lambda land, slot: land.at[slot, pl.ds(rows[0], rows[1] - rows[0])]
    return view


def _view_block(src, slot):
    return src.at[slot]


W_IN_SHARD = W_IN_REF // N_DEV


def _view_window(rows):
    def view(src, slot):
        col0 = pl.multiple_of((W_IN_SHARD * slot // DH) * DH, DH)
        return src.at[pl.ds(rows[0], rows[1] - rows[0]), pl.ds(col0, D)]
    return view


def _split_copies(view, srcs, lands, send_sems, recv_sems, local_sems):
    x, y, c = _place()
    me = _slot((x, y, c))
    into = getattr(view, "land", lambda land, slot: land.at[slot])
    local, sends, waits = [], [], []
    for a, (src, land) in enumerate(zip(srcs, lands)):
        local.append(pltpu.make_async_copy(view(src, me), into(land, me), local_sems.at[a]))
        for k in getattr(view, "peers", range(1, N_DEV)):
            peer = _peer_of(x, y, c, k)
            mine = view(src, _slot(peer))
            sems = dict(send_sem=send_sems.at[N_REL * a + k - 1], recv_sem=recv_sems.at[N_REL * a + k - 1],
                        device_id=peer, device_id_type=MESH)
            sends.append(pltpu.make_async_remote_copy(src_ref=mine, dst_ref=into(land, me), **sems))
            waits.append(pltpu.make_async_remote_copy(src_ref=mine, dst_ref=into(land, _slot(peer)), **sems))
    return local, sends, waits


def _split_start(view, land_shapes, srcs, name, after, lands=None):
    n = len(srcs)
    if lands is None:
        lands = [lax.empty(shp, s.dtype) for shp, s in zip(land_shapes, srcs)]

    def body(*refs):
        src_refs, land_refs = refs[:n], refs[n:2 * n]
        send_sems, recv_sems, local_sems = refs[2 * n + 1:2 * n + 4]
        token = refs[-1]
        local, sends, _ = _split_copies(view, src_refs, land_refs, send_sems, recv_sems, local_sems)
        for cp in local + sends:
            cp.start()
        token[...] = jnp.zeros_like(token)

    hbm = lambda a: pltpu.with_memory_space_constraint(a, pltpu.HBM)
    out = pl.pallas_call(
        body, name=name,
        out_shape=(pltpu.SemaphoreType.DMA((N_REL * n,)), pltpu.SemaphoreType.DMA((N_REL * n,)),
                   pltpu.SemaphoreType.DMA((n,)),
                   *[pltpu.HBM(s.shape, s.dtype) for s in srcs], *[pltpu.HBM(l.shape, l.dtype) for l in lands],
                   jax.ShapeDtypeStruct((8, DH), F32)),
        in_specs=[HBM] * (2 * n) + [ANY],
        out_specs=(SEM, SEM, SEM, *([HBM] * (2 * n)), pl.BlockSpec(memory_space=pltpu.VMEM)),
        input_output_aliases={i: 3 + i for i in range(2 * n)},
        compiler_params=pltpu.CompilerParams(has_side_effects=EFFECT),
    )(*[hbm(s) for s in srcs], *[hbm(l) for l in lands], after)
    handle = dict(view=view, n=n, sems=out[:3], srcs=list(out[3:3 + n]), lands=list(out[3 + n:3 + 2 * n]))
    return handle, out[-1]


def _split_wait(handle, name, after, srcs=None, lands=None):
    view, n, sems = handle["view"], handle["n"], handle["sems"]
    srcs = handle["srcs"] if srcs is None else srcs
    lands = handle["lands"] if lands is None else lands
    afters = list(after) if isinstance(after, (list, tuple)) else [after]

    def body(*refs):
        src_refs, land_refs = refs[:n], refs[n:2 * n]
        send_sems, recv_sems, local_sems = refs[2 * n:2 * n + 3]
        local, _, waits = _split_copies(view, src_refs, land_refs, send_sems, recv_sems, local_sems)
        for cp in waits:
            cp.wait_send()
            cp.wait_recv()
        for cp in local:
            cp.wait()

    out = pl.pallas_call(
        body, name=name,
        out_shape=(*[pltpu.HBM(s.shape, s.dtype) for s in srcs], *[pltpu.HBM(l.shape, l.dtype) for l in lands]),
        in_specs=[HBM] * (2 * n) + [SEM, SEM, SEM] + [ANY] * len(afters),
        out_specs=tuple([HBM] * (2 * n)),
        input_output_aliases={i: i for i in range(2 * n)},
        compiler_params=pltpu.CompilerParams(has_side_effects=EFFECT),
    )(*srcs, *lands, *sems, *afters)
    handle["srcs"] = list(out[:n])
    return list(out[n:])


def _tie(x, token, name):
    def body(x_ref, t_ref, o_ref):
        pass

    return pl.pallas_call(
        body, name=name, out_shape=jax.ShapeDtypeStruct(x.shape, x.dtype),
        in_specs=[ANY, ANY], out_specs=ANY, input_output_aliases={0: 0},
    )(x, token)


def _forward_to_sibling(land, name):
    def body(land_ref, out_ref, send_sems, recv_sems):
        x, y, c = _place()
        sibling = (x, y, 1 - c)
        chips = [(1 - x, y), (x, 1 - y), (1 - x, 1 - y)]

        def copy(j, core):
            blk = _slot((*chips[j], core))
            return pltpu.make_async_remote_copy(src_ref=land_ref.at[blk], dst_ref=out_ref.at[blk],
                                                send_sem=send_sems.at[j], recv_sem=recv_sems.at[j],
                                                device_id=sibling, device_id_type=MESH)

        sends = [copy(j, c) for j in range(3)]
        for cp in sends:
            cp.start()
        for j in range(3):
            copy(j, 1 - c).wait_recv()
        for cp in sends:
            cp.wait_send()

    return pl.pallas_call(
        body, name=name, in_specs=[ANY], out_specs=ANY, input_output_aliases={0: 0},
        out_shape=jax.ShapeDtypeStruct(land.shape, land.dtype),
        scratch_shapes=[pltpu.SemaphoreType.DMA((3,)), pltpu.SemaphoreType.DMA((3,))],
    )(land)


def _mod_fwd(a, w, b):
    def body(a_ref, w_ref, b_ref, o_ref):
        o_ref[...] = _dot(_silu(a_ref[...]), w_ref[...], NN, precision=HI) + b_ref[...]

    return pl.pallas_call(
        body, name="mod_fwd", out_shape=jax.ShapeDtypeStruct((a.shape[0], w.shape[1]), F32),
        compiler_params=pltpu.CompilerParams(vmem_limit_bytes=VMEM_LIMIT),
    )(a, w, b)


def _mod_bwd(a, d, w):
    def body(a_ref, d_ref, w_ref, dw_ref, dc_ref):
        av = a_ref[...]
        dv = d_ref[...]
        dw_ref[...] = _dot(_silu(av), dv, TN, precision=HI)
        da = _dot(dv[0:8, :], w_ref[...], NT, precision=HI) * _dsilu(av[0:8, :])
        row = lax.broadcasted_iota(jnp.int32, da.shape, 0)
        dc_ref[...] = jnp.where(row == 0, da, 0.0)

    return pl.pallas_call(
        body, name="mod_bwd",
        out_shape=[jax.ShapeDtypeStruct(w.shape, F32), jax.ShapeDtypeStruct((8, w.shape[0]), F32)],
        compiler_params=pltpu.CompilerParams(vmem_limit_bytes=VMEM_LIMIT),
    )(a, d, w)


def _sum_devices(g):
    def body(g_ref, o_ref):
        acc = g_ref[0]
        for i in range(1, g.shape[0]):
            acc = acc + g_ref[i]
        o_ref[...] = acc

    return pl.pallas_call(body, name="sum_devices_%d" % g.shape[1],
                          out_shape=jax.ShapeDtypeStruct(g.shape[1:], F32))(g)


def _sum_windows(g, name):
    n, r, c = g.shape
    tr = 128

    def body(g_ref, o_ref):
        x, y, cc = _place()
        lane0 = (W_IN_SHARD * _slot((x, y, cc))) % DH
        acc = g_ref[0].astype(F32)
        for i in range(1, n):
            acc = acc + g_ref[i].astype(F32)
        o_ref[...] = pltpu.roll(acc, (c - lane0) % c, 1).T

    return pl.pallas_call(
        body, name=name, grid=(r // tr,),
        in_specs=[pl.BlockSpec((n, tr, c), lambda i: (0, i, 0))],
        out_specs=pl.BlockSpec((c, tr), lambda i: (0, i)),
        out_shape=jax.ShapeDtypeStruct((c, r), F32),
        compiler_params=_cp("parallel"),
    )(g)


def _adam_rows(r, c, n):
    budget = 6 * 1024 * 1024
    best = None
    for tr in range(16, r + 1, 16):
        if r % tr == 0 and tr * c * (2 * n + 28) <= budget:
            best = tr
    return best if best is not None else r


def _adamw(g, w, m, v, name):
    n, r, c = g.shape
    tr = _adam_rows(r, c, n)
    bc1 = 1.0 - ADAM_B1 ** ADAM_STEP
    bc2 = 1.0 - ADAM_B2 ** ADAM_STEP

    def body(g_ref, w_ref, m_ref, v_ref, go_ref, d_ref, mo_ref, vo_ref):
        grad = g_ref[0].astype(F32)
        for i in range(1, n):
            grad = grad + g_ref[i].astype(F32)
        go_ref[...] = grad
        m_new = ADAM_B1 * m_ref[...] + (1.0 - ADAM_B1) * grad
        v_new = ADAM_B2 * v_ref[...] + (1.0 - ADAM_B2) * (grad * grad)
        mo_ref[...] = m_new
        vo_ref[...] = v_new
        d_ref[...] = -ADAM_LR * ((m_new / bc1) / (jnp.sqrt(v_new / bc2) + ADAM_EPS) + ADAM_WD * w_ref[...])

    blk = pl.BlockSpec((tr, c), lambda i: (i, 0))
    out = jax.ShapeDtypeStruct((r, c), F32)
    return pl.pallas_call(
        body, name=name, grid=(r // tr,),
        in_specs=[pl.BlockSpec((n, tr, c), lambda i: (0, i, 0)), blk, blk, blk],
        out_specs=[blk] * 4, out_shape=[out] * 4,
        compiler_params=_cp("parallel"),
    )(g, w, m, v)


ADAM_ROWS3 = 168


def _adam_math(grad, w, m, v):
    bc1 = 1.0 - ADAM_B1 ** ADAM_STEP
    bc2 = 1.0 - ADAM_B2 ** ADAM_STEP
    m_new = ADAM_B1 * m + (1.0 - ADAM_B1) * grad
    v_new = ADAM_B2 * v + (1.0 - ADAM_B2) * (grad * grad)
    delta = -ADAM_LR * ((m_new / bc1) / (jnp.sqrt(v_new / bc2) + ADAM_EPS) + ADAM_WD * w)
    return delta, m_new, v_new


def _adamw_rows3(g, w3, m3, v3, name):
    r, _, c = w3.shape
    n = ADAM_ROWS3
    starts = list(range(0, r - n, n)) + [r - n]

    def body(g_hbm, w_hbm, m_hbm, v_hbm, go_hbm, d_hbm, mo_hbm, vo_hbm, gbuf, ibuf, obuf, in_sems, out_sems):
        def fetch(p):
            r0, slot = starts[p], p % 2
            g0 = (r0 // 8) * 8
            cps = [pltpu.make_async_copy(g_hbm.at[pl.ds(g0, n + 8)], gbuf.at[slot], in_sems.at[slot, 0])]
            cps += [pltpu.make_async_copy(h.at[pl.ds(r0, n), 0], ibuf.at[slot, k], in_sems.at[slot, 1 + k])
                    for k, h in enumerate((w_hbm, m_hbm, v_hbm))]
            for cp in cps:
                cp.start()
            return cps

        pending, outs = fetch(0), []
        for p, r0 in enumerate(starts):
            slot = p % 2
            nxt = fetch(p + 1) if p + 1 < len(starts) else []
            for cp in pending:
                cp.wait()
            grad = gbuf[slot, pl.ds(r0 - (r0 // 8) * 8, n), :]
            delta, m_new, v_new = _adam_math(grad, ibuf[slot, 0], ibuf[slot, 1], ibuf[slot, 2])
            for cp in outs:
                cp.wait()
            for k, val in enumerate((grad, delta, m_new, v_new)):
                obuf[slot, k] = val
            outs = [pltpu.make_async_copy(obuf.at[slot, k], h.at[pl.ds(r0, n), 0], out_sems.at[slot, k])
                    for k, h in enumerate((go_hbm, d_hbm, mo_hbm, vo_hbm))]
            for cp in outs:
                cp.start()
            pending = nxt
        for cp in outs:
            cp.wait()

    out = jax.ShapeDtypeStruct(w3.shape, F32)
    return pl.pallas_call(
        body, name=name, in_specs=[ANY] * 4, out_specs=[ANY] * 4, out_shape=[out] * 4,
        scratch_shapes=[pltpu.VMEM((2, n + 8, c), F32), pltpu.VMEM((2, 3, n, c), F32), pltpu.VMEM((2, 4, n, c), F32),
                        pltpu.SemaphoreType.DMA((2, 4)), pltpu.SemaphoreType.DMA((2, 4))],
        compiler_params=pltpu.CompilerParams(vmem_limit_bytes=VMEM_LIMIT),
    )(g, w3, m3, v3)


def kernel(x, c, ctx, c_ctx, w_mod, b_mod, norm_pre1, norm_post1, norm_pre2, norm_post2, w_in, hg_lb, hg_onorm, gla_w_gk, gla_b_gk, gla_onorm, w_br_hg, w_br_gla, w_out, w_ff_gate, w_ff_up, w_ff_down, loss_target, m_c_ctx, m_w_mod, m_b_mod, m_norm_pre1, m_norm_post1, m_norm_pre2, m_norm_post2, m_w_in, m_hg_lb, m_hg_onorm, m_gla_w_gk, m_gla_b_gk, m_gla_onorm, m_w_br_hg, m_w_br_gla, m_w_out, m_w_ff_gate, m_w_ff_up, m_w_ff_down, v_c_ctx, v_w_mod, v_b_mod, v_norm_pre1, v_norm_post1, v_norm_pre2, v_norm_post2, v_w_in, v_hg_lb, v_hg_onorm, v_gla_w_gk, v_gla_b_gk, v_gla_onorm, v_w_br_hg, v_w_br_gla, v_w_out, v_w_ff_gate, v_w_ff_up, v_w_ff_down):
    xi, yi, ci = lax.axis_index("x"), lax.axis_index("y"), lax.axis_index("c")
    me = 4 * xi + 2 * yi + ci
    t = CTX + x.shape[1]

    w_in_pieces, w_in_state = [], {}

    def start_w_in(i, after):
        rows = (i * W_IN_PIECE, (i + 1) * W_IN_PIECE)
        handle, tok = _split_start(_view_near_rows(rows), None, w_in_state["src"], "ag_w_in_start%d" % i, after,
                                   lands=w_in_state["land"])
        w_in_state.update(src=handle["srcs"], land=handle["lands"])
        w_in_pieces.append(handle)
        return tok

    tr_ = lambda a: jnp.swapaxes(a[0], 0, 1)
    w_in_bf = jnp.pad(w_in[0].astype(BF16), ((0, 0), (0, W_IN_PAD - W_IN_SHARD)))
    w_in_state.update(src=[w_in_bf], land=[lax.empty((N_DEV,) + w_in_bf.shape, BF16)])
    gathered = lambda arrs: [(N_DEV,) + a.shape for a in arrs]
    tok = start_w_in(0, c)
    small_in = [c, hg_lb, gla_w_gk[0], gla_b_gk[0]]
    small_handle, tok = _split_start(_view_whole, gathered(small_in), small_in, "ag_small_start", tok)
    tok = start_w_in(1, tok)
    c_all, lb_g, wgk_g, bgk_g = _split_wait(small_handle, "ag_small_wait", tok)
    big = [w_in[0], w_br_hg[0], w_br_gla[0], w_out[0], tr_(w_ff_gate), tr_(w_ff_up), w_ff_down[0]]
    big_bf = [None] + [w.astype(BF16) for w in big[1:]]
    cols = lambda g: jnp.transpose(g, (1, 0, 2)).reshape(g.shape[1], N_DEV * g.shape[2])

    def get_w_in(after):
        for i, handle in enumerate(w_in_pieces):
            land = _split_wait(handle, "ag_w_in_wait%d" % i, after, srcs=w_in_state["src"], lands=w_in_state["land"])
            w_in_state.update(src=handle["srcs"], land=land)
        return _assemble_w_in(_forward_to_sibling(w_in_state["land"][0], "ag_w_in_forward"))

    def get_mix(after):
        g_brh, g_brg, g_out = _split_wait(mix_handle, "ag_mix_wait", after)
        return _gate_cols(cols(g_brh)), _gate_cols(cols(g_brg)), _gate_rows(g_out.reshape(D, D))

    def get_ffn(after):
        g_gate, g_up, g_down = _split_wait(ffn_handle, "ag_ffn_wait", after)
        return (g_gate.reshape(D_FF, D), g_up.reshape(D_FF, D)), g_down.reshape(D_FF, D)

    hg_lb_full = jnp.transpose(lb_g, (1, 2, 0, 3)).reshape(2, 2, HW)
    wgk_k = _layout_wgk(jnp.transpose(wgk_g, (1, 2, 0, 3)).reshape(2, 16, HW)).astype(BF16)
    bgk_k = jnp.transpose(bgk_g, (1, 0, 2)).reshape(1, D)
    onw = jnp.concatenate([jnp.tile(hg_onorm, (1, NH // 2)), jnp.tile(gla_onorm, (1, NH // 2))], axis=1)

    n_mod = w_mod.shape[2]
    a9 = jnp.concatenate([c_ctx[None], c_all[:, 0], jnp.zeros((16 - 1 - N_DEV, D), F32)], axis=0)
    b_loc = lax.dynamic_slice(b_mod, (0, me * n_mod), (1, n_mod))
    s_loc = _mod_fwd(a9, w_mod[0], b_loc)
    mod_handle, tok = _split_start(_view_whole, gathered([s_loc]), [s_loc], "ag_mod_start", s_loc)
    tok = start_w_in(2, tok)
    tok = start_w_in(3, tok)
    s_all, = _split_wait(mod_handle, "ag_mod_wait", tok)
    mod_all = jnp.transpose(s_all, (1, 0, 2)).reshape(16, N_DEV * n_mod)
    pad8 = lambda m: jnp.concatenate([m.reshape(6, D), jnp.zeros((2, D), F32)], axis=0)
    modc = pad8(mod_all[0])
    modx = pad8(lax.dynamic_slice(mod_all, (1 + me, 0), (1, N_DEV * n_mod))[0])

    mix_handle, tok = _split_start(_view_whole, gathered(big_bf[1:4]), big_bf[1:4], "ag_mix_start", s_all)
    ffn_handle, tok = _split_start(_view_whole, gathered(big_bf[4:]), big_bf[4:], "ag_ffn_start", tok)

    z = (ctx[0], x[0])
    modx = _tie(modx, tok, "tie_mod")
    norms = (norm_pre1, norm_post1, norm_pre2, norm_post2)
    shard = lambda d: jnp.transpose(d.reshape(d.shape[0], N_DEV, -1), (1, 0, 2)).astype(BF16)
    rowshard = lambda d: d.reshape(N_DEV, d.shape[0] // N_DEV, d.shape[1]).astype(BF16)
    sent, w_in_grad = [], {}

    def send_w_in(i, x_after):
        half, rows = W_IN_GRAD_CHUNKS[i]
        handle, tok = _split_start(_view_window(rows), [(N_DEV, rows[1] - rows[0], D)], w_in_grad[half],
                                   "grads_w_in%d_start" % i, x_after)
        w_in_grad[half] = handle["srcs"]
        sent.append(("w_in%d" % i, ["w_in#%d" % i], handle))
        return tok

    def send(names, grads, x_after):
        if names == ("w_in_a",):
            w_in_grad["a"] = list(grads)
            return _tie(x_after, send_w_in(0, x_after), "tie_w_in0")
        if names == ("w_in_b",):
            w_in_grad["b"] = list(grads)
            return x_after
        arrs, leaves = [], []
        for nm, g in zip(names, grads):
            if nm in ("w_gate_t", "w_up_t"):
                arrs.append(rowshard(g))
                leaves.append({"w_gate_t": "w_ff_gate", "w_up_t": "w_ff_up"}[nm])
            elif nm == "w_down":
                arrs.append(rowshard(g))
                leaves.append("w_ff_down")
            elif nm == "w_out":
                arrs.append(rowshard(g[GOFF:GOFF + D]))
                leaves.append(nm)
            else:
                arrs.append(shard(g[:, GOFF:GOFF + D]))
                leaves.append(nm)
        handle, tok = _split_start(_view_block, [a.shape for a in arrs], arrs, "grads_%s_start" % names[0], x_after)
        sent.append((names[0], leaves, handle))
        return _tie(x_after, tok, "tie_" + names[0])

    r = _local_step(z, loss_target[0], modc, modx, norms, onw, hg_lb_full, wgk_k, bgk_k,
                    get_w_in, get_mix, get_ffn, send)
    grad_x = r["grad_x"][None]

    sm_pre, sm_mid, sm_fin = r["sm_pre"], r["sm_mid"], r["sm_final"]
    dmodc = jnp.stack([sm_pre[0], sm_pre[2], sm_mid[4], sm_mid[0], sm_mid[2], sm_fin[0]]).reshape(-1)
    dmodx = jnp.stack([sm_pre[1], sm_pre[3], sm_mid[5], sm_mid[1], sm_mid[3], sm_fin[1]]).reshape(-1)
    on = r["sm_post"][0].reshape(NH, DH)
    pieces = [dmodc, dmodx, sm_pre[4], sm_mid[7], sm_mid[6], sm_fin[2], on[:NH // 2].sum(0), on[NH // 2:].sum(0),
              r["d_lb"][:2].reshape(-1), _unlayout_wgk(r["d_wgk"]).reshape(-1), r["d_bgk"][0]]
    loss_local = (0.5 / D) * jnp.sum(r["loss_vec"])
    pieces.append(jnp.concatenate([loss_local.reshape(1), jnp.zeros((DH - 1,), F32)]))
    sizes = [p.shape[0] for p in pieces]
    pack = jnp.concatenate(pieces).reshape(-1, DH)
    moms = [(m_w_in, v_w_in), (m_w_br_hg, v_w_br_hg), (m_w_br_gla, v_w_br_gla), (m_w_out, v_w_out),
            (m_w_ff_gate, v_w_ff_gate), (m_w_ff_up, v_w_ff_up), (m_w_ff_down, v_w_ff_down)]
    names = ["w_in", "w_br_hg", "w_br_gla", "w_out", "w_ff_gate", "w_ff_up", "w_ff_down"]
    wmv = {nm: (w, m, v) for nm, w, (m, v) in zip(names, big, moms)}
    res = {}

    def update(nm):
        w, m, v = wmv[nm]
        if nm in ("w_ff_gate", "w_ff_up"):
            outs = _adamw(recv[nm], w, tr_(m), tr_(v), "adamw_" + nm)
            res[nm] = [jnp.swapaxes(o, 0, 1)[None] for o in outs]
        else:
            res[nm] = [o[None] for o in _adamw(recv[nm], w, m[0], v[0], "adamw_" + nm)]

    small_handle, tok = _split_start(_view_whole, [(N_DEV,) + pack.shape], [pack], "small_grads_start", pack)
    tok = send_w_in(1, tok)
    recv = {}
    for first, leaves, handle in sent:
        if not first.startswith("w_in"):
            recv.update(zip(leaves, _split_wait(handle, "grads_%s_wait" % first, tok)))
    update("w_ff_gate")
    update("w_ff_up")
    pack_all, = _split_wait(small_handle, "small_grads_wait", [res["w_ff_gate"][0], res["w_ff_up"][0]])
    tot = _sum_devices(pack_all).reshape(-1)
    offs = [sum(sizes[:i]) for i in range(len(sizes))]
    part = lambda i: tot[offs[i]:offs[i] + sizes[i]]
    dmodc_t, dmodx_t = part(0), part(1)
    g_b_mod = (dmodc_t + dmodx_t)[None]
    g_norms = [part(i)[None] for i in (2, 3, 4, 5)]
    g_hg_on, g_gla_on = part(6)[None], part(7)[None]
    lb0 = lax.dynamic_slice(part(8).reshape(2, HW), (0, me * (HW // N_DEV)), (2, HW // N_DEV))
    g_hg_lb = jnp.stack([lb0, -lb0])
    g_wgk = lax.dynamic_slice(part(9).reshape(2, 16, HW), (0, 0, me * (HW // N_DEV)), (2, 16, HW // N_DEV))[None]
    g_bgk = lax.dynamic_slice(part(10).reshape(2, HW), (0, me * (HW // N_DEV)), (2, HW // N_DEV))[None]
    loss = part(11)[0]

    dmx_all = pack_all.reshape(N_DEV, -1)[:, sizes[0]:sizes[0] + sizes[1]]
    d9 = jnp.concatenate([lax.dynamic_slice(dmodc_t[None], (0, me * n_mod), (1, n_mod)),
                          lax.dynamic_slice(dmx_all, (0, me * n_mod), (N_DEV, n_mod)),
                          jnp.zeros((16 - 1 - N_DEV, n_mod), F32)], axis=0)
    g_w_mod, dcc_part = _mod_bwd(a9, d9, w_mod[0])
    cctx_handle, tok = _split_start(_view_whole, [(N_DEV,) + dcc_part.shape], [dcc_part], "c_ctx_start", dcc_part)
    tok = send_w_in(2, tok)
    recv["w_ff_down"] = _tie(recv["w_ff_down"], tok, "tie_down")
    update("w_ff_down")
    res["w_mod"] = [o[None] for o in _adamw(g_w_mod[None], w_mod[0], m_w_mod[0], v_w_mod[0], "adamw_w_mod")]
    for nm in ("w_out", "w_br_hg", "w_br_gla"):
        update(nm)
    dcc_all, = _split_wait(cctx_handle, "c_ctx_wait", [res["w_ff_down"][0], res["w_mod"][0]])
    g_c_ctx = _sum_devices(dcc_all)[0]

    small = [("c_ctx", c_ctx, m_c_ctx, v_c_ctx, g_c_ctx), ("b_mod", b_mod, m_b_mod, v_b_mod, g_b_mod),
             ("norm_pre1", norm_pre1, m_norm_pre1, v_norm_pre1, g_norms[0]),
             ("norm_post1", norm_post1, m_norm_post1, v_norm_post1, g_norms[1]),
             ("norm_pre2", norm_pre2, m_norm_pre2, v_norm_pre2, g_norms[2]),
             ("norm_post2", norm_post2, m_norm_post2, v_norm_post2, g_norms[3]),
             ("hg_lb", hg_lb, m_hg_lb, v_hg_lb, g_hg_lb), ("hg_onorm", hg_onorm, m_hg_onorm, v_hg_onorm, g_hg_on),
             ("gla_w_gk", gla_w_gk, m_gla_w_gk, v_gla_w_gk, g_wgk), ("gla_b_gk", gla_b_gk, m_gla_b_gk, v_gla_b_gk, g_bgk),
             ("gla_onorm", gla_onorm, m_gla_onorm, v_gla_onorm, g_gla_on)]
    flat = lambda k: jnp.concatenate([s[k].reshape(-1) for s in small]).reshape(-1, DH)
    outs = _adamw(flat(4)[None], flat(1), flat(2), flat(3), "adamw_small")
    off = 0
    for nm, w, _, _, _ in small:
        res[nm] = [o.reshape(-1)[off:off + w.size].reshape(w.shape) for o in outs]
        off += w.size

    done = [res[nm][0] for nm in names[1:]] + [res["w_mod"][0], outs[0]]
    sums = []
    for i, (first, leaves, handle) in enumerate(s for s in sent if s[0].startswith("w_in")):
        half = W_IN_GRAD_CHUNKS[i][0]
        land, = _split_wait(handle, "grads_%s_wait" % first, done, srcs=w_in_grad[half])
        w_in_grad[half] = handle["srcs"]
        sums.append(_sum_windows(land, "sum_windows%d" % i))
    major = lambda a: jnp.transpose(a, (2, 0, 1))
    outs = _adamw_rows3(jnp.concatenate(sums, axis=1), major(w_in), major(m_w_in), major(v_w_in), "adamw_w_in")
    res["w_in"] = [jnp.transpose(o, (1, 2, 0)) for o in outs]

    order = ["c_ctx", "w_mod", "b_mod", "norm_pre1", "norm_post1", "norm_pre2", "norm_post2", "w_in", "hg_lb",
             "hg_onorm", "gla_w_gk", "gla_b_gk", "gla_onorm", "w_br_hg", "w_br_gla", "w_out", "w_ff_gate", "w_ff_up",
             "w_ff_down"]
    return (loss, grad_x, *[res[n][k] for k in range(4) for n in order])
```

```python
import functools

import jax
import jax.numpy as jnp
from jax import lax
from jax.experimental import pallas as pl
from jax.experimental.pallas import tpu as pltpu

F32 = jnp.float32
BF16 = jnp.bfloat16
HI = lax.Precision.HIGHEST

N_DEV = 8
D = 1024
CTX = 256
HW = 512
DH = 128
NH = 8
D_FF = 2816
EPS = 1e-6
GLA_NORM = 16.0
CHUNK = 64
TR = 256
NCT = CTX // TR
W_IN_COLS = 7168
MAIN0 = 0
LR0 = 4608
GW = 1152
GOFF = 32
GATE_HG0 = LR0
GATE_GLA0 = LR0 + D
LEVELS = (32, 16, 8)
EXP_CLAMP = 80.0
VMEM_LIMIT = 48 * 1024 * 1024

ADAM_LR, ADAM_B1, ADAM_B2, ADAM_EPS, ADAM_WD, ADAM_STEP = 0.001, 0.9, 0.999, 1e-08, 0.01, 10


def _cp(*sem):
    return pltpu.CompilerParams(dimension_semantics=sem, vmem_limit_bytes=VMEM_LIMIT)


def _sig(x):
    return jax.nn.sigmoid(x)


def _silu(x):
    return x * _sig(x)


def _dsilu(x):
    s = _sig(x)
    return s * (1.0 + x * (1.0 - s))


def _rstd(x):
    return lax.rsqrt(jnp.mean(x * x, axis=-1, keepdims=True) + EPS)


def _rms_bwd(a, y, r):
    return r * (a - y * (r * r) * jnp.mean(a * y, axis=-1, keepdims=True))


def _colsum(x):
    return jnp.sum(x, axis=0, keepdims=True)


def _dot(a, b, dims, precision=None):
    return lax.dot_general(a, b, (dims, ((), ())), preferred_element_type=F32, precision=precision)


NN = ((1,), (0,))
NT = ((1,), (1,))
TN = ((0,), (0,))

SCAN_HEADS_FWD = 4
SCAN_HEADS_BWD = 4


def _split_dot(m, x):
    mb = m.astype(BF16)
    x1 = x.astype(BF16)
    r1 = x - x1.astype(F32)
    x2 = r1.astype(BF16)
    x3 = (r1 - x2.astype(F32)).astype(BF16)
    return _dot(mb, x1, NN) + _dot(mb, x2, NN) + _dot(mb, x3, NN)


def _matmul(a, b, dims, out_dtype, name, tm, tn, tk, a_off=0, m_out=None):
    a_pair = isinstance(a, (tuple, list))
    as_ = list(a) if a_pair else [a]
    a = as_[0]
    pair = isinstance(b, (tuple, list))
    bs = list(b) if pair else [b]
    b1 = bs[0]
    rows = b1.shape[0] * len(bs)
    half = None
    if dims == NN:
        m, k, n = a.shape[0], rows, b1.shape[1]
        a_spec = pl.BlockSpec((tm, tk), lambda i, j, kk: (i, kk + a_off))
        half = b1.shape[0] // tk
        if a_pair:
            assert pair and a.shape[1] == b1.shape[0] and a_off == 0
            a_spec = [pl.BlockSpec((tm, tk), lambda i, j, kk: (i, jnp.minimum(kk, half - 1))),
                      pl.BlockSpec((tm, tk), lambda i, j, kk: (i, jnp.maximum(kk - half, 0)))]
        b_maps = [lambda i, j, kk: (kk, j)] if not pair else [
            lambda i, j, kk: (jnp.minimum(kk, half - 1), j), lambda i, j, kk: (jnp.maximum(kk - half, 0), j)]
        b_specs = [pl.BlockSpec((tk, tn), f) for f in b_maps]
        axis = 2
    elif dims == NT:
        m, k, n = a.shape[0], b1.shape[1], rows
        a_spec = pl.BlockSpec((tm, tk), lambda i, j, kk: (i, kk + a_off))
        half = b1.shape[0] // tn
        b_maps = [lambda i, j, kk: (j, kk)] if not pair else [
            lambda i, j, kk: (jnp.minimum(j, half - 1), kk), lambda i, j, kk: (jnp.maximum(j - half, 0), kk)]
        b_specs = [pl.BlockSpec((tn, tk), f) for f in b_maps]
        axis = 1
    else:
        assert not pair
        m, k = (a.shape[1] if m_out is None else m_out), a.shape[0]
        n = b1.shape[1]
        a_spec = pl.BlockSpec((tk, tm), lambda i, j, kk: (kk, i + a_off))
        b_specs = [pl.BlockSpec((tk, tn), lambda i, j, kk: (kk, j))]
    assert m % tm == 0 and n % tn == 0 and k % tk == 0, (name, m, n, k, tm, tn, tk)
    nk = k // tk
    nb = len(bs)
    na = len(as_)
    assert na == 1 or dims == NN

    def body(*refs):
        a_refs, refs = refs[:na], refs[na:]
        o_ref = refs[nb]
        if pair:
            bv = jnp.where(pl.program_id(axis) < half, refs[0][...], refs[1][...])
        else:
            bv = refs[0][...]
        av = a_refs[0][...] if na == 1 else jnp.where(pl.program_id(2) < half, a_refs[0][...], a_refs[1][...])
        part = _dot(av, bv, dims)
        if nk == 1:
            o_ref[...] = part.astype(o_ref.dtype)
            return
        acc_ref = refs[nb + 1]
        kk = pl.program_id(2)

        @pl.when(kk == 0)
        def _():
            acc_ref[...] = part

        @pl.when(kk > 0)
        def _():
            acc_ref[...] += part

        @pl.when(kk == nk - 1)
        def _():
            o_ref[...] = acc_ref[...].astype(o_ref.dtype)

    return pl.pallas_call(
        body,
        name=name,
        grid=(m // tm, n // tn, nk),
        in_specs=(a_spec if a_pair else [a_spec]) + b_specs,
        out_specs=pl.BlockSpec((tm, tn), lambda i, j, kk: (i, j)),
        out_shape=jax.ShapeDtypeStruct((m, n), out_dtype),
        scratch_shapes=[] if nk == 1 else [pltpu.VMEM((tm, tn), F32)],
        compiler_params=_cp("parallel", "parallel", "arbitrary"),
    )(*as_, *bs)


def _mm_gu_act(h, w_gate_t, w_up_t, name, tm):
    t = h.shape[0]
    tn = D_FF // 2

    def body(a_ref, bg_ref, bu_ref, u_ref, v_ref, act_ref):
        a = a_ref[...]
        u = _dot(a, bg_ref[...], NT)
        v = _dot(a, bu_ref[...], NT)
        u_ref[...] = u.astype(BF16)
        v_ref[...] = v.astype(BF16)
        act_ref[...] = (_silu(u) * v).astype(BF16)

    wspec = pl.BlockSpec((tn, D), lambda i, j: (j, 0))
    ospec = pl.BlockSpec((tm, tn), lambda i, j: (i, j))
    out = jax.ShapeDtypeStruct((t, D_FF), BF16)
    return pl.pallas_call(
        body, name=name, grid=(t // tm, D_FF // tn),
        in_specs=[pl.BlockSpec((tm, D), lambda i, j: (i, 0)), wspec, wspec],
        out_specs=[ospec] * 3, out_shape=[out] * 3,
        compiler_params=_cp("parallel", "parallel"),
    )(h, w_gate_t, w_up_t)


def _mm_down_dx_act(dy, w_down, u, v, name, tm):
    t = dy.shape[0]
    tn = D_FF // 2

    def body(a_ref, b_ref, u_ref, v_ref, du_ref, dv_ref):
        dact = _dot(a_ref[...], b_ref[...], NT)
        u = u_ref[...].astype(F32)
        du_ref[...] = (dact * v_ref[...].astype(F32) * _dsilu(u)).astype(BF16)
        dv_ref[...] = (dact * _silu(u)).astype(BF16)

    ospec = pl.BlockSpec((tm, tn), lambda i, j: (i, j))
    out = jax.ShapeDtypeStruct((t, D_FF), BF16)
    return pl.pallas_call(
        body, name=name, grid=(t // tm, D_FF // tn),
        in_specs=[pl.BlockSpec((tm, D), lambda i, j: (i, 0)), pl.BlockSpec((tn, D), lambda i, j: (j, 0)), ospec, ospec],
        out_specs=[ospec] * 2, out_shape=[out] * 2,
        compiler_params=_cp("parallel", "parallel"),
    )(dy, w_down, u, v)


def _row(c):
    return pl.BlockSpec((TR, c), lambda i: (i, 0))


def _rowcol(width, cb):
    return pl.BlockSpec((TR, width), lambda i: (i, cb))


def _full(shape):
    return pl.BlockSpec(shape, lambda i: (0,) * len(shape))


def _mod_row(mc_ref, mx_ref, k, is_ctx):
    return jnp.where(is_ctx, mc_ref[k:k + 1, :], mx_ref[k:k + 1, :])


def _z_specs():
    return [pl.BlockSpec((TR, D), lambda i: (jnp.minimum(i, NCT - 1), 0)),
            pl.BlockSpec((TR, D), lambda i: (jnp.maximum(i - NCT, 0), 0))]


def _z_tile(c_ref, x_ref, is_ctx):
    return jnp.where(is_ctx, c_ref[...], x_ref[...])


def _acc_row(ref, k, val):
    ref[k:k + 1, :] += val


def _acc_mod(ref, k, is_ctx, val):
    zero = jnp.zeros_like(val)
    ref[k:k + 1, :] += jnp.where(is_ctx, val, zero)
    ref[k + 1:k + 2, :] += jnp.where(is_ctx, zero, val)


def _prenorm(z, nw, modc, modx, i_shift, i_scale, name):
    t = z[0].shape[0] + z[1].shape[0]

    def body(zc_ref, zx_ref, nw_ref, mc_ref, mx_ref, h_ref):
        is_ctx = pl.program_id(0) < NCT
        x = _z_tile(zc_ref, zx_ref, is_ctx)
        n = x * _rstd(x) * nw_ref[...]
        h = n * (1.0 + _mod_row(mc_ref, mx_ref, i_scale, is_ctx)) + _mod_row(mc_ref, mx_ref, i_shift, is_ctx)
        h_ref[...] = h.astype(BF16)

    return pl.pallas_call(
        body, name=name, grid=(t // TR,),
        in_specs=_z_specs() + [_full((1, D)), _full((8, D)), _full((8, D))],
        out_specs=_row(D),
        out_shape=jax.ShapeDtypeStruct((t, D), BF16),
        compiler_params=_cp("parallel"),
    )(*z, nw, modc, modx)


def _hg_lb(lb_ref, d):
    a0 = lb_ref[0, d:d + 1, :]
    a1 = lb_ref[1, d:d + 1, :]
    mx = jnp.maximum(a0, a1)
    e0 = jnp.exp(a0 - mx)
    e1 = jnp.exp(a1 - mx)
    return e0 / (e0 + e1)


def _log_sigmoid(x):
    return jnp.minimum(x, 0.0) - jnp.log(1.0 + jnp.exp(-jnp.abs(x)))


def _gates_fwd(p, hg_lb, wgk, bgk):
    t = p.shape[0]
    seg = lambda j: _rowcol(HW, MAIN0 // HW + j)

    def body(hq_ref, hi_ref, hf_ref, hb_ref, gq_ref, gk_ref, gv_ref, lr_ref, lb_ref, wgk_ref, bgk_ref,
             q_ref, v_ref, kf_ref, kb_ref, gf_ref, gb_ref):
        q_ref[:, :HW] = _silu(hq_ref[...].astype(F32)).astype(BF16)
        q_ref[:, HW:] = (gq_ref[...].astype(F32) * (DH ** -0.5)).astype(BF16)
        v_ref[:, :HW] = hi_ref[...]
        v_ref[:, HW:] = gv_ref[...]
        xg = _dot(lr_ref[...].astype(BF16), wgk_ref[...], NN) + bgk_ref[...]
        for d, (raw_ref, k_ref, g_ref) in enumerate(((hf_ref, kf_ref, gf_ref), (hb_ref, kb_ref, gb_ref))):
            lbd = _hg_lb(lb_ref, d)
            f = lbd + (1.0 - lbd) * _sig(raw_ref[...].astype(F32))
            k_ref[:, :HW] = (1.0 - f).astype(BF16)
            k_ref[:, HW:] = gk_ref[...]
            g_ref[:, :HW] = jnp.log(f)
            g_ref[:, HW:] = _log_sigmoid(xg[:, d * HW:(d + 1) * HW]) * (1.0 / GLA_NORM)

    out = jax.ShapeDtypeStruct((t, D), F32)
    outb = jax.ShapeDtypeStruct((t, D), BF16)
    return pl.pallas_call(
        body, name="gates_fwd", grid=(t // TR,),
        in_specs=[seg(0), seg(1), seg(2), seg(3), seg(5), seg(6), seg(7), _rowcol(DH, LR0 // DH),
                  _full((2, 2, HW)), _full((DH, D)), _full((1, D))],
        out_specs=[_row(D)] * 6,
        out_shape=[outb] * 4 + [out] * 2,
        compiler_params=_cp("parallel"),
    )(p, p, p, p, p, p, p, p, hg_lb, wgk, bgk)


def _post_fwd(o_fw, o_bw, p, onw):
    t = o_fw.shape[0]

    def body(of_ref, ob_ref, g1_ref, g2_ref, w_ref, y_ref):
        for h in range(NH):
            sl = slice(h * DH, (h + 1) * DH)
            o = of_ref[:, sl] + ob_ref[:, sl]
            g_ref = g1_ref if h < NH // 2 else g2_ref
            gs = slice((h % (NH // 2)) * DH, (h % (NH // 2) + 1) * DH)
            n = o * _rstd(o) * w_ref[:, sl]
            y_ref[:, sl] = (n * _silu(g_ref[:, gs].astype(F32))).astype(BF16)

    return pl.pallas_call(
        body, name="post_fwd", grid=(t // TR,),
        in_specs=[_row(D), _row(D), _rowcol(HW, MAIN0 // HW + 4), _rowcol(HW, MAIN0 // HW + 8), _full((1, D))],
        out_specs=_row(D),
        out_shape=jax.ShapeDtypeStruct((t, D), BF16),
        compiler_params=_cp("parallel"),
    )(o_fw, o_bw, p, p, onw)


def _gate_window_specs(col0):
    return [_rowcol(HW, col0 // HW), _rowcol(HW, col0 // HW + 1), _rowcol(DH, (col0 + 2 * HW) // DH)]


def _gate_window(refs):
    return jnp.concatenate([r[...].astype(F32) for r in refs], axis=1)


def _branch_merge(y, w_hg, w_gla, p):
    t = y.shape[0]

    def body(y_ref, wh_ref, wg_ref, a0, a1, a2, b0, b1, b2, u1_ref, u2_ref, m_ref):
        u1 = _dot(y_ref[:, :HW], wh_ref[...], NN)
        u2 = _dot(y_ref[:, HW:], wg_ref[...], NN)
        u1_ref[...] = u1.astype(BF16)
        u2_ref[...] = u2.astype(BF16)
        m_ref[...] = (_sig(_gate_window((a0, a1, a2))) * u1 + _sig(_gate_window((b0, b1, b2))) * u2).astype(BF16)

    out = jax.ShapeDtypeStruct((t, GW), BF16)
    return pl.pallas_call(
        body, name="branch_merge", grid=(t // TR,),
        in_specs=[_row(D), _full((HW, GW)), _full((HW, GW))] + _gate_window_specs(GATE_HG0)
        + _gate_window_specs(GATE_GLA0),
        out_specs=[_row(GW)] * 3, out_shape=[out] * 3,
        compiler_params=_cp("parallel"),
    )(y, w_hg, w_gla, p, p, p, p, p, p)


def _mid_fwd(z, y1, nw_post, nw_pre, modc, modx):
    t = y1.shape[0]

    def body(zc_ref, zx_ref, y_ref, wpo_ref, wpr_ref, mc_ref, mx_ref, z1_ref, h_ref):
        is_ctx = pl.program_id(0) < NCT
        y = y_ref[...].astype(F32)
        z1 = _z_tile(zc_ref, zx_ref, is_ctx) + _mod_row(mc_ref, mx_ref, 2, is_ctx) * (y * _rstd(y) * wpo_ref[...])
        z1_ref[...] = z1
        n = z1 * _rstd(z1) * wpr_ref[...]
        h = n * (1.0 + _mod_row(mc_ref, mx_ref, 4, is_ctx)) + _mod_row(mc_ref, mx_ref, 3, is_ctx)
        h_ref[...] = h.astype(BF16)

    return pl.pallas_call(
        body, name="mid_fwd", grid=(t // TR,),
        in_specs=_z_specs() + [_row(D), _full((1, D)), _full((1, D)), _full((8, D)), _full((8, D))],
        out_specs=[_row(D), _row(D)],
        out_shape=[jax.ShapeDtypeStruct((t, D), F32), jax.ShapeDtypeStruct((t, D), BF16)],
        compiler_params=_cp("parallel"),
    )(*z, y1, nw_post, nw_pre, modc, modx)


def _final(z1, y2, target, nw, modc, modx):
    t = z1.shape[0]

    def body(z1_ref, y_ref, tg_ref, w_ref, mc_ref, mx_ref, dz_ref, dy_ref, loss_ref, sm_ref):
        i = pl.program_id(0)
        is_ctx = i < NCT

        @pl.when(i == 0)
        def _():
            loss_ref[...] = jnp.zeros_like(loss_ref)
            sm_ref[...] = jnp.zeros_like(sm_ref)

        g = _mod_row(mc_ref, mx_ref, 5, is_ctx)
        y = y_ref[...].astype(F32)
        r = _rstd(y)
        w = w_ref[...]
        yr = y * r
        n = yr * w
        e = z1_ref[...] + g * n - tg_ref[...]
        lat = jnp.where(is_ctx, 0.0, 1.0)
        loss_ref[...] += lat * _colsum(e * e)
        dz = e * (lat / D)
        dz_ref[...] = dz
        _acc_mod(sm_ref, 0, is_ctx, _colsum(dz * n))
        dn = dz * g
        _acc_row(sm_ref, 2, _colsum(dn * yr))
        dy_ref[...] = _rms_bwd(dn * w, y, r).astype(BF16)

    return pl.pallas_call(
        body, name="final", grid=(t // TR,),
        in_specs=[_row(D), _row(D), pl.BlockSpec((TR, D), lambda i: (jnp.maximum(i - NCT, 0), 0)),
                  _full((1, D)), _full((8, D)), _full((8, D))],
        out_specs=[_row(D), _row(D), _full((1, D)), _full((8, D))],
        out_shape=[jax.ShapeDtypeStruct((t, D), F32), jax.ShapeDtypeStruct((t, D), BF16),
                   jax.ShapeDtypeStruct((1, D), F32), jax.ShapeDtypeStruct((8, D), F32)],
        compiler_params=_cp("arbitrary"),
    )(z1, y2, target, nw, modc, modx)


def _mid_bwd(dh2, dz, z1, y1, nw_post, nw_pre, modc, modx):
    t = z1.shape[0]

    def body(dh_ref, dz_ref, z1_ref, y_ref, wpo_ref, wpr_ref, mc_ref, mx_ref, dzo_ref, dy_ref, sm_ref):
        i = pl.program_id(0)
        is_ctx = i < NCT

        @pl.when(i == 0)
        def _():
            sm_ref[...] = jnp.zeros_like(sm_ref)

        dh = dh_ref[...].astype(F32)
        z1 = z1_ref[...]
        r = _rstd(z1)
        zr = z1 * r
        wpr = wpr_ref[...]
        n = zr * wpr
        _acc_mod(sm_ref, 0, is_ctx, _colsum(dh))
        _acc_mod(sm_ref, 2, is_ctx, _colsum(dh * n))
        dn = dh * (1.0 + _mod_row(mc_ref, mx_ref, 4, is_ctx))
        _acc_row(sm_ref, 6, _colsum(dn * zr))
        dz1 = dz_ref[...] + _rms_bwd(dn * wpr, z1, r)
        dzo_ref[...] = dz1
        y = y_ref[...].astype(F32)
        r1 = _rstd(y)
        yr = y * r1
        wpo = wpo_ref[...]
        g = _mod_row(mc_ref, mx_ref, 2, is_ctx)
        _acc_mod(sm_ref, 4, is_ctx, _colsum(dz1 * (yr * wpo)))
        dn1 = dz1 * g
        _acc_row(sm_ref, 7, _colsum(dn1 * yr))
        dy_ref[...] = _rms_bwd(dn1 * wpo, y, r1).astype(BF16)

    return pl.pallas_call(
        body, name="mid_bwd", grid=(t // TR,),
        in_specs=[_row(D)] * 4 + [_full((1, D)), _full((1, D)), _full((8, D)), _full((8, D))],
        out_specs=[_row(D), _row(D), _full((8, D))],
        out_shape=[jax.ShapeDtypeStruct((t, D), F32), jax.ShapeDtypeStruct((t, D), BF16),
                   jax.ShapeDtypeStruct((8, D), F32)],
        compiler_params=_cp("arbitrary"),
    )(dh2, dz, z1, y1, nw_post, nw_pre, modc, modx)


def _pre_bwd(dh1, dz, z, nw, modc, modx):
    t = dh1.shape[0]

    def body(dh_ref, dz_ref, zc_ref, zx_ref, w_ref, mc_ref, mx_ref, dzo_ref, sm_ref):
        i = pl.program_id(0)
        is_ctx = i < NCT

        @pl.when(i == 0)
        def _():
            sm_ref[...] = jnp.zeros_like(sm_ref)

        dh = dh_ref[...].astype(F32)
        x = _z_tile(zc_ref, zx_ref, is_ctx)
        r = _rstd(x)
        xr = x * r
        w = w_ref[...]
        _acc_mod(sm_ref, 0, is_ctx, _colsum(dh))
        _acc_mod(sm_ref, 2, is_ctx, _colsum(dh * (xr * w)))
        dn = dh * (1.0 + _mod_row(mc_ref, mx_ref, 1, is_ctx))
        _acc_row(sm_ref, 4, _colsum(dn * xr))
        dzo_ref[...] = dz_ref[...] + _rms_bwd(dn * w, x, r)

    return pl.pallas_call(
        body, name="pre_bwd", grid=(t // TR,),
        in_specs=[_row(D)] * 2 + _z_specs() + [_full((1, D)), _full((8, D)), _full((8, D))],
        out_specs=[pl.BlockSpec((TR, D), lambda i: (jnp.maximum(i - NCT, 0), 0)), _full((8, D))],
        out_shape=[jax.ShapeDtypeStruct((t - CTX, D), F32), jax.ShapeDtypeStruct((8, D), F32)],
        compiler_params=_cp("arbitrary"),
    )(dh1, dz, *z, nw, modc, modx)


def _branch_merge_bwd(dm, p, u1, u2, w_hg, w_gla):
    t = dm.shape[0]

    def body(dm_ref, a0, a1, a2, b0, b1, b2, u1_ref, u2_ref, wh_ref, wg_ref, du1_ref, du2_ref, dg_ref, dyh_ref, dyg_ref):
        dm_ = dm_ref[...].astype(F32)
        s1 = _sig(_gate_window((a0, a1, a2)))
        s2 = _sig(_gate_window((b0, b1, b2)))
        du1 = (dm_ * s1).astype(BF16)
        du2 = (dm_ * s2).astype(BF16)
        du1_ref[...] = du1
        du2_ref[...] = du2
        dg_ref[:, :GW] = (dm_ * u1_ref[...].astype(F32) * s1 * (1.0 - s1)).astype(BF16)
        dg_ref[:, GW:] = (dm_ * u2_ref[...].astype(F32) * s2 * (1.0 - s2)).astype(BF16)
        dyh_ref[...] = _dot(du1, wh_ref[...], NT).astype(BF16)
        dyg_ref[...] = _dot(du2, wg_ref[...], NT).astype(BF16)

    return pl.pallas_call(
        body, name="branch_merge_bwd", grid=(t // TR,),
        in_specs=[_row(GW)] + _gate_window_specs(GATE_HG0) + _gate_window_specs(GATE_GLA0)
        + [_row(GW), _row(GW), _full((HW, GW)), _full((HW, GW))],
        out_specs=[_row(GW), _row(GW), _row(2 * GW), _row(HW), _row(HW)],
        out_shape=[jax.ShapeDtypeStruct((t, GW), BF16), jax.ShapeDtypeStruct((t, GW), BF16),
                   jax.ShapeDtypeStruct((t, 2 * GW), BF16), jax.ShapeDtypeStruct((t, HW), BF16),
                   jax.ShapeDtypeStruct((t, HW), BF16)],
        compiler_params=_cp("parallel"),
    )(dm, p, p, p, p, p, p, u1, u2, w_hg, w_gla)


def _post_bwd(dy_hg, dy_gla, o_fw, o_bw, p, onw):
    t = o_fw.shape[0]

    def body(d1_ref, d2_ref, of_ref, ob_ref, g1_ref, g2_ref, w_ref, do_ref, dg_ref, sm_ref):
        @pl.when(pl.program_id(0) == 0)
        def _():
            sm_ref[...] = jnp.zeros_like(sm_ref)

        for h in range(NH):
            sl = slice(h * DH, (h + 1) * DH)
            gs = slice((h % (NH // 2)) * DH, (h % (NH // 2) + 1) * DH)
            g_ref, d_ref = (g1_ref, d1_ref) if h < NH // 2 else (g2_ref, d2_ref)
            o = of_ref[:, sl] + ob_ref[:, sl]
            r = _rstd(o)
            orr = o * r
            w = w_ref[:, sl]
            gt = g_ref[:, gs].astype(F32)
            dy = d_ref[:, gs].astype(F32)
            dg_ref[:, sl] = (dy * (orr * w) * _dsilu(gt)).astype(BF16)
            dn = dy * _silu(gt)
            sm_ref[0:1, sl] += _colsum(dn * orr)
            do_ref[:, sl] = _rms_bwd(dn * w, o, r)

    return pl.pallas_call(
        body, name="post_bwd", grid=(t // TR,),
        in_specs=[_row(HW), _row(HW), _row(D), _row(D), _rowcol(HW, MAIN0 // HW + 4), _rowcol(HW, MAIN0 // HW + 8),
                  _full((1, D))],
        out_specs=[_row(D), _row(D), _full((8, D))],
        out_shape=[jax.ShapeDtypeStruct((t, D), F32), jax.ShapeDtypeStruct((t, D), BF16),
                   jax.ShapeDtypeStruct((8, D), F32)],
        compiler_params=_cp("arbitrary"),
    )(dy_hg, dy_gla, o_fw, o_bw, p, p, onw)


def _gates_bwd(p, hg_lb, wgk, bgk, dgm, dgo, dq_f, dq_b, dv_f, dv_b, dk_f, dk_b, dg_f, dg_b):
    t = p.shape[0]
    seg = lambda j: _rowcol(HW, MAIN0 // HW + j)

    def body(hq_ref, hf_ref, hb_ref, lr_ref, lb_ref, wgk_ref, bgk_ref, dgm_ref, dgo_ref,
             dqf_ref, dqb_ref, dvf_ref, dvb_ref, dkf_ref, dkb_ref, dgf_ref, dgb_ref,
             dp_ref, dlb_ref, dw_ref, db_ref):
        @pl.when(pl.program_id(0) == 0)
        def _():
            dlb_ref[...] = jnp.zeros_like(dlb_ref)
            dw_ref[...] = jnp.zeros_like(dw_ref)
            db_ref[...] = jnp.zeros_like(db_ref)

        c0 = MAIN0

        def put(j, val):
            dp_ref[:, c0 + j * HW:c0 + (j + 1) * HW] = val.astype(BF16)

        dq = dqf_ref[...].astype(F32) + dqb_ref[...].astype(F32)
        dv = dvf_ref[...].astype(F32) + dvb_ref[...].astype(F32)
        put(0, dq[:, :HW] * _dsilu(hq_ref[...].astype(F32)))
        put(1, dv[:, :HW])
        put(5, dq[:, HW:] * (DH ** -0.5))
        put(7, dv[:, HW:])
        put(6, dkf_ref[:, HW:].astype(F32) + dkb_ref[:, HW:].astype(F32))
        dp_ref[:, c0 + 4 * HW:c0 + 5 * HW] = dgo_ref[:, :HW]
        dp_ref[:, c0 + 8 * HW:c0 + 9 * HW] = dgo_ref[:, HW:]
        lr = lr_ref[...].astype(BF16)
        xg = _dot(lr, wgk_ref[...], NN) + bgk_ref[...]
        dxg = []
        for d, (raw_ref, dk_ref, dg_ref) in enumerate(((hf_ref, dkf_ref, dgf_ref), (hb_ref, dkb_ref, dgb_ref))):
            lbd = _hg_lb(lb_ref, d)
            s = _sig(raw_ref[...].astype(F32))
            f = lbd + (1.0 - lbd) * s
            df = dg_ref[:, :HW] / f - dk_ref[:, :HW].astype(F32)
            put(2 + d, df * (1.0 - lbd) * s * (1.0 - s))
            dlb_ref[d:d + 1, :] += _colsum(df * (1.0 - s)) * (lbd * (1.0 - lbd))
            dxg.append(dg_ref[:, HW:] * (1.0 / GLA_NORM) * _sig(-xg[:, d * HW:(d + 1) * HW]))
        dxg = jnp.concatenate(dxg, axis=1)
        db_ref[0:1, :] += _colsum(dxg)
        dxg_b = dxg.astype(BF16)
        dw_ref[...] += _dot(lr, dxg_b, TN)
        dlr = _dot(dxg_b, wgk_ref[...], NT)
        dp_ref[:, LR0:LR0 + DH] = (dlr + dgm_ref[:, :DH].astype(F32)).astype(BF16)
        dp_ref[:, LR0 + DH:GATE_GLA0] = dgm_ref[:, DH:D]
        dp_ref[:, GATE_GLA0:GATE_GLA0 + DH] = dgm_ref[:, D:GW] + dgm_ref[:, GW:GW + DH]
        dp_ref[:, GATE_GLA0 + DH:GATE_GLA0 + GW] = dgm_ref[:, GW + DH:]
        dp_ref[:, GATE_GLA0 + GW:] = jnp.zeros((TR, W_IN_COLS - GATE_GLA0 - GW), BF16)

    return pl.pallas_call(
        body, name="gates_bwd", grid=(t // TR,),
        in_specs=[seg(0), seg(2), seg(3), _rowcol(DH, LR0 // DH), _full((2, 2, HW)), _full((DH, D)), _full((1, D)),
                  _row(2 * GW), _row(D)] + [_row(D)] * 8,
        out_specs=[_row(W_IN_COLS), _full((8, HW)), _full((DH, D)), _full((8, D))],
        out_shape=[jax.ShapeDtypeStruct((t, W_IN_COLS), BF16), jax.ShapeDtypeStruct((8, HW), F32),
                   jax.ShapeDtypeStruct((DH, D), F32), jax.ShapeDtypeStruct((8, D), F32)],
        compiler_params=_cp("arbitrary"),
    )(p, p, p, p, hg_lb, wgk, bgk, dgm, dgo, dq_f, dq_b, dv_f, dv_b, dk_f, dk_b, dg_f, dg_b)


def _scan_consts(rev):
    r = lax.broadcasted_iota(jnp.int32, (CHUNK, CHUNK), 0)
    u = lax.broadcasted_iota(jnp.int32, (CHUNK, CHUNK), 1)
    rp = lax.broadcasted_iota(jnp.int32, (CHUNK, 1), 0)
    if rev:
        r, u, rp = CHUNK - 1 - r, CHUNK - 1 - u, CHUNK - 1 - rp
    tri = jnp.where(u <= r, 1.0, 0.0).astype(F32)
    tri_t = jnp.where(r <= u, 1.0, 0.0).astype(F32)
    lv = []
    for b in LEVELS:
        sh = b.bit_length() - 1
        pair = ((r >> sh) == (u >> sh) + 1) & (((u >> sh) & 1) == 0)
        pair_t = ((u >> sh) == (r >> sh) + 1) & (((r >> sh) & 1) == 0)
        tside = ((rp >> sh) & 1) == 1
        lv.append((pair, pair_t, tside, jnp.where(tside, 1.0, -1.0).astype(F32)))
    bd = LEVELS[-1].bit_length() - 1
    diag = ((r >> bd) == (u >> bd)) & (u <= r)
    diag_t = ((r >> bd) == (u >> bd)) & (r <= u)
    return tri, tri_t, lv, diag, diag_t


def _row_of(pos, rev):
    return CHUNK - 1 - pos if rev else pos


def _chunk_terms(cum, b_scr, consts, rev):
    _, _, lv, _, _ = consts
    terms = []
    for b, (_, _, _, sgn) in zip(LEVELS, lv):
        pieces = []
        for j in range(CHUNK // (2 * b)):
            row = _row_of(2 * b * j + b - 1, rev)
            pieces.append(jnp.broadcast_to(b_scr[row:row + 1, :], (2 * b, DH)))
        if rev:
            pieces = pieces[::-1]
        bnd = pieces[0] if len(pieces) == 1 else jnp.concatenate(pieces, axis=0)
        terms.append(jnp.exp((cum - bnd) * sgn))
    b = LEVELS[-1]
    pieces = []
    for j in range(CHUNK // b):
        if j == 0:
            pieces.append(jnp.zeros((b, DH), F32))
        else:
            row = _row_of(b * j - 1, rev)
            pieces.append(jnp.broadcast_to(b_scr[row:row + 1, :], (b, DH)))
    if rev:
        pieces = pieces[::-1]
    start = jnp.concatenate(pieces, axis=0)
    wq = jnp.exp(jnp.minimum(cum - start, 0.0))
    wk = jnp.exp(jnp.minimum(start - cum, EXP_CLAMP))
    terms.append((wq, wk))
    return terms


def _run_staged(units):
    live = list(units)
    while live:
        nxt = []
        for u in live:
            try:
                next(u)
                nxt.append(u)
            except StopIteration:
                pass
        live = nxt


SCAN_TB = 256
SCAN_CB = SCAN_TB // CHUNK


def _block_order(i, ntb, rev):
    nctx = CTX // SCAN_TB
    if not rev:
        return i
    return jnp.where(i < nctx, nctx - 1 - i, ntb - 1 - (i - nctx))


def _chunk_in_block(j, rev):
    return SCAN_CB - 1 - j if rev else j


def _scan_fwd(q, k, v, g, rev):
    t = q.shape[0]
    nc = t // CHUNK
    hpb = SCAN_HEADS_FWD

    def body(q_ref, k_ref, v_ref, g_ref, o_ref, st_ref, s_scr, b_scr):
        consts = _scan_consts(rev)
        _, _, lv, diag, _ = consts
        masks = [lvl[0] for lvl in lv] + [diag]

        @pl.when(pl.program_id(1) == 0)
        def _():
            s_scr[...] = jnp.zeros_like(s_scr)

        tri = consts[0]
        state = {hh: s_scr[hh] for hh in range(hpb)}

        def unit(hh, j):
            sl = slice(hh * DH, (hh + 1) * DH)
            c = _chunk_in_block(j, rev)
            rows = slice(c * CHUNK, (c + 1) * CHUNK)
            b_ref = b_scr.at[hh * SCAN_CB + j]
            qc, kc, vc, gc = q_ref[rows, sl], k_ref[rows, sl], v_ref[rows, sl], g_ref[rows, sl]
            cum = _split_dot(tri, gc)
            b_ref[...] = cum
            yield
            terms = _chunk_terms(cum, b_ref, consts, rev)
            qf, kf = qc.astype(F32), kc.astype(F32)
            xs = [(jnp.where(tside, qf, kf) * w).astype(BF16) for w, (_, _, tside, _) in zip(terms[:-1], lv)]
            qd, kd = (qf * terms[-1][0]).astype(BF16), (kf * terms[-1][1]).astype(BF16)
            tot = _colsum(gc)
            qe = (qf * jnp.exp(cum)).astype(BF16)
            ke = (kf * jnp.exp(tot - cum)).astype(BF16)
            vb = vc.astype(BF16)
            yield
            scs = [_dot(x, x, NT) for x in xs] + [_dot(qd, kd, NT)]
            kv = _dot(vb, ke, TN)
            yield
            a = jnp.zeros((CHUNK, CHUNK), F32)
            for sc, m in zip(scs, masks):
                a = a + jnp.where(m, sc, 0.0)
            o_intra = _dot(a.astype(BF16), vb, NN)
            yield
            st = state[hh]
            st_ref[hh, c] = st
            o_ref[rows, sl] = o_intra + _dot(qe, st.astype(BF16), NT)
            state[hh] = st * jnp.exp(tot) + kv
            yield

        _run_staged([unit(hh, j) for hh in range(hpb) for j in range(SCAN_CB)])
        for hh in range(hpb):
            s_scr[hh] = state[hh]

    ntb = t // SCAN_TB
    col = pl.BlockSpec((SCAN_TB, hpb * DH), lambda h, i: (_block_order(i, ntb, rev), h))
    return pl.pallas_call(
        body, name="scan_fwd_" + ("bw" if rev else "fw"), grid=(NH // hpb, ntb),
        in_specs=[col] * 4,
        out_specs=[col, pl.BlockSpec((hpb, SCAN_CB, DH, DH), lambda h, i: (h, _block_order(i, ntb, rev), 0, 0))],
        out_shape=[jax.ShapeDtypeStruct((t, D), F32), jax.ShapeDtypeStruct((NH, nc, DH, DH), F32)],
        scratch_shapes=[pltpu.VMEM((hpb, DH, DH), F32), pltpu.VMEM((hpb * SCAN_CB, CHUNK, DH), F32)],
        compiler_params=_cp("parallel", "arbitrary"),
    )(q, k, v, g)


def _scan_bwd(q, k, v, g, do, states, rev):
    t = q.shape[0]
    nc = t // CHUNK
    hpb = SCAN_HEADS_BWD

    def body(q_ref, k_ref, v_ref, g_ref, do_ref, st_ref, dq_ref, dk_ref, dv_ref, dg_ref, ds_scr, b_scr):
        consts = _scan_consts(rev)
        _, tri_t, lv, diag, diag_t = consts
        masks = [(lvl[0], lvl[1]) for lvl in lv] + [(diag, diag_t)]
        @pl.when(pl.program_id(1) == 0)
        def _():
            ds_scr[...] = jnp.zeros_like(ds_scr)

        tri = consts[0]
        dstate = {hh: ds_scr[hh] for hh in range(hpb)}

        def unit(hh, jj):
            sl = slice(hh * DH, (hh + 1) * DH)
            c = _chunk_in_block(SCAN_CB - 1 - jj, rev)
            rows = slice(c * CHUNK, (c + 1) * CHUNK)
            b_ref = b_scr.at[hh * SCAN_CB + jj]
            qc, kc, vc, gc = q_ref[rows, sl], k_ref[rows, sl], v_ref[rows, sl], g_ref[rows, sl]
            dob = do_ref[rows, sl].astype(BF16)
            vb = vc.astype(BF16)
            cum = _split_dot(tri, gc)
            b_ref[...] = cum
            da = _dot(dob, vb, NT)
            da_t = _dot(vb, dob, NT)
            yield
            terms = _chunk_terms(cum, b_ref, consts, rev)
            qf, kf = qc.astype(F32), kc.astype(F32)
            xs = [(jnp.where(tside, qf, kf) * w).astype(BF16) for w, (_, _, tside, _) in zip(terms[:-1], lv)]
            wqd, wkd = terms[-1]
            qdb, kdb = (qf * wqd).astype(BF16), (kf * wkd).astype(BF16)
            tot = _colsum(gc)
            e_tot = jnp.exp(tot)
            e_b = jnp.exp(cum)
            e_t = jnp.exp(tot - cum)
            qeb = (qf * e_b).astype(BF16)
            keb = (kf * e_t).astype(BF16)
            dsym = [(jnp.where(m, da, 0.0) + jnp.where(m_t, da_t, 0.0)).astype(BF16) for m, m_t in masks[:-1]]
            dad = (jnp.where(diag, da, 0.0).astype(BF16), jnp.where(diag_t, da_t, 0.0).astype(BF16))
            yield
            sym = [_dot(x, x, NT) for x in xs]
            dxs = [_dot(d, x, NN) for d, x in zip(dsym, xs)]
            at_d = _dot(kdb, qdb, NT)
            dqt_d = _dot(dad[0], kdb, NN)
            dkt_d = _dot(dad[1], qdb, NN)
            qd = _dot(dob, qeb, TN)
            yield
            a_t = jnp.where(diag_t, at_d, 0.0)
            dq = dqt_d * wqd
            dk = dkt_d * wkd
            db = dqt_d * qdb.astype(F32) - dkt_d * kdb.astype(F32)
            for s, dx, x, w, (_, m_t, tside, sgn) in zip(sym, dxs, xs, terms[:-1], lv):
                a_t = a_t + jnp.where(m_t, s, 0.0)
                dxw = dx * w
                dq = dq + jnp.where(tside, dxw, 0.0)
                dk = dk + jnp.where(tside, 0.0, dxw)
                db = db + (dx * x.astype(F32)) * sgn
            dv_intra = _dot(a_t.astype(BF16), dob, NN)
            st = st_ref[hh, c]
            stb = st.astype(BF16)
            dqe = _dot(dob, stb, NN)
            yield
            dst = dstate[hh]
            dstb = dst.astype(BF16)
            dstate[hh] = dst * e_tot + qd
            dv_ref[rows, sl] = (dv_intra + _dot(keb, dstb, NT)).astype(BF16)
            dke = _dot(vb, dstb, NN)
            yield
            qe = qeb.astype(F32)
            ke = keb.astype(F32)
            dq_ref[rows, sl] = (dq + dqe * e_b).astype(BF16)
            dk_ref[rows, sl] = (dk + dke * e_t).astype(BF16)
            db = db + dqe * qe - dke * ke
            dtot = _colsum(dstb.astype(F32) * stb.astype(F32)) * e_tot + _colsum(dke * ke)
            dg_ref[rows, sl] = _split_dot(tri_t, db) + dtot
            yield

        _run_staged([unit(hh, jj) for hh in range(hpb) for jj in range(SCAN_CB)])
        for hh in range(hpb):
            ds_scr[hh] = dstate[hh]

    ntb = t // SCAN_TB
    blk = lambda i: _block_order(ntb - 1 - i, ntb, rev)
    col = pl.BlockSpec((SCAN_TB, hpb * DH), lambda h, i: (blk(i), h))
    out = jax.ShapeDtypeStruct((t, D), F32)
    outb = jax.ShapeDtypeStruct((t, D), BF16)
    return pl.pallas_call(
        body, name="scan_bwd_" + ("bw" if rev else "fw"), grid=(NH // hpb, ntb),
        in_specs=[col] * 5 + [pl.BlockSpec((hpb, SCAN_CB, DH, DH), lambda h, i: (h, blk(i), 0, 0))],
        out_specs=[col] * 4,
        out_shape=[outb] * 3 + [out],
        scratch_shapes=[pltpu.VMEM((hpb, DH, DH), F32), pltpu.VMEM((hpb * SCAN_CB, CHUNK, DH), F32)],
        compiler_params=_cp("parallel", "arbitrary"),
    )(q, k, v, g, do, states)


W_IN_GRAD_CHUNKS = (("a", (0, 512)), ("b", (0, 256)), ("b", (256, 512)))
W_IN_REF = 6688
W_IN_PAD = 896
W_IN_PIECE = 512


def _assemble_w_in(g):
    n, r, wp = g.shape
    tr = 256
    tiles = wp // DH

    def body(g_ref, o_ref):
        lane = lax.broadcasted_iota(jnp.int32, (tr, DH), 1)
        for t in range(W_IN_COLS // DH):
            acc = None
            for j in range(n):
                c = DH * t - W_IN_SHARD * j
                if c <= -DH or c >= W_IN_SHARD:
                    continue
                k, s = divmod(c, DH)
                lo = g_ref[j, :, k * DH:(k + 1) * DH] if 0 <= k < tiles else None
                hi = g_ref[j, :, (k + 1) * DH:(k + 2) * DH] if s and 0 <= k + 1 < tiles else None
                if s:
                    zero = jnp.zeros((tr, DH), g.dtype)
                    lo = zero if lo is None else pltpu.roll(lo, DH - s, 1)
                    hi = zero if hi is None else pltpu.roll(hi, DH - s, 1)
                    part = jnp.where(lane < DH - s, lo, hi)
                else:
                    part = lo
                acc = part if acc is None else acc + part
            o_ref[:, t * DH:(t + 1) * DH] = jnp.zeros((tr, DH), g.dtype) if acc is None else acc

    return pl.pallas_call(
        body, name="assemble_w_in", grid=(r // tr,),
        in_specs=[pl.BlockSpec((n, tr, wp), lambda i: (0, i, 0))],
        out_specs=pl.BlockSpec((tr, W_IN_COLS), lambda i: (i, 0)),
        out_shape=jax.ShapeDtypeStruct((r, W_IN_COLS), g.dtype),
        compiler_params=_cp("parallel"),
    )(g)


def _gate_cols(w):
    return jnp.pad(w, ((0, 0), (GOFF, GW - GOFF - D)))


def _gate_rows(w):
    return jnp.pad(w, ((GOFF, GW - GOFF - D), (0, 0)))


def _layout_wgk(w):
    r = w.shape[1]
    top = jnp.concatenate([w[0], jnp.zeros_like(w[0])], axis=1)
    bot = jnp.concatenate([jnp.zeros_like(w[1]), w[1]], axis=1)
    return jnp.concatenate([top, bot, jnp.zeros((DH - 2 * r, D), w.dtype)], axis=0)


def _unlayout_wgk(d, r=16):
    return jnp.stack([d[:r, :HW], d[r:2 * r, HW:]])


def _local_step(z, target, modc, modx, norms, onw, hg_lb, wgk, bgk, get_w_in, get_mix, get_ffn, send):
    n_pre1, n_post1, n_pre2, n_post2 = norms
    t = z[0].shape[0] + z[1].shape[0]
    tm = 1152 if t % 1152 == 0 else 256
    h1 = _prenorm(z, n_pre1, modc, modx, 0, 1, "prenorm1")
    w_in = get_w_in(h1)
    p = _matmul(h1, w_in, NN, BF16, "mm_in", t, 1024, D)
    q, v, k_f, k_b, g_f, g_b = _gates_fwd(p, hg_lb, wgk, bgk)
    o_f, st_f = _scan_fwd(q, k_f, v, g_f, False)
    o_b, st_b = _scan_fwd(q, k_b, v, g_b, True)
    y = _post_fwd(o_f, o_b, p, onw)
    w_br_hg, w_br_gla, w_out = get_mix(y)
    u1, u2, merged = _branch_merge(y, w_br_hg, w_br_gla, p)
    y1 = _matmul(merged, w_out, NN, BF16, "mm_out", tm, 512, GW)
    z1, h2 = _mid_fwd(z, y1, n_post1, n_pre2, modc, modx)
    w_gu_t, w_down = get_ffn(h2)
    u, v_ff, act = _mm_gu_act(h2, w_gu_t[0], w_gu_t[1], "mm_gu", tm)
    y2 = _matmul(act, w_down, NN, BF16, "mm_down", t, 512, D_FF)
    dz, dy2, loss_vec, sm_final = _final(z1, y2, target, n_post2, modc, modx)
    du, dv_ff = _mm_down_dx_act(dy2, w_down, u, v_ff, "mm_down_dx", tm)
    d_w_down = _matmul(act, dy2, TN, BF16, "mm_down_dw", D_FF // 2, 1024, t)
    dh2 = _matmul((du, dv_ff), w_gu_t, NN, BF16, "mm_gu_dx", tm, 512, D_FF)
    d_w_gate_t = _matmul(du, h2, TN, BF16, "mm_gate_dw", D_FF // 2, 1024, t)
    d_w_up_t = _matmul(dv_ff, h2, TN, BF16, "mm_up_dw", D_FF // 2, 1024, t)
    dh2 = send(("w_down", "w_gate_t", "w_up_t"), (d_w_down, d_w_gate_t, d_w_up_t), dh2)
    dz, dy1, sm_mid = _mid_bwd(dh2, dz, z1, y1, n_post1, n_pre2, modc, modx)
    dmerged = _matmul(dy1, w_out, NT, BF16, "mm_out_dx", tm, GW, D)
    d_w_out = _matmul(merged, dy1, TN, BF16, "mm_out_dw", GW, 512, t)
    du1, du2, dgm, dy_hg, dy_gla = _branch_merge_bwd(dmerged, p, u1, u2, w_br_hg, w_br_gla)
    d_w_br_hg = _matmul(y, du1, TN, BF16, "mm_br_hg_dw", HW, GW, t, a_off=0, m_out=HW)
    d_w_br_gla = _matmul(y, du2, TN, BF16, "mm_br_gla_dw", HW, GW, t, a_off=1, m_out=HW)
    dy_hg = send(("w_out", "w_br_hg", "w_br_gla"), (d_w_out, d_w_br_hg, d_w_br_gla), dy_hg)
    do, dgo, sm_post = _post_bwd(dy_hg, dy_gla, o_f, o_b, p, onw)
    dq_f, dk_f, dv_f, dg_f = _scan_bwd(q, k_f, v, g_f, do, st_f, False)
    dq_b, dk_b, dv_b, dg_b = _scan_bwd(q, k_b, v, g_b, do, st_b, True)
    dp, d_lb, d_wgk, d_bgk = _gates_bwd(p, hg_lb, wgk, bgk, dgm, dgo, dq_f, dq_b, dv_f, dv_b, dk_f, dk_b, dg_f, dg_b)
    d_w_in_a = _matmul(h1, dp, TN, BF16, "mm_in_dw_a", 512, 1024, t, a_off=0, m_out=D // 2)
    dp = send(("w_in_a",), (d_w_in_a,), dp)
    d_w_in_b = _matmul(h1, dp, TN, BF16, "mm_in_dw_b", 512, 1024, t, a_off=1, m_out=D // 2)
    dp = send(("w_in_b",), (d_w_in_b,), dp)
    dh1 = _matmul(dp, w_in, NT, BF16, "mm_in_dx", tm, 512, W_IN_COLS // 2)
    grad_x, sm_pre = _pre_bwd(dh1, dz, z, n_pre1, modc, modx)
    return dict(loss_vec=loss_vec, grad_x=grad_x, sm_final=sm_final, sm_mid=sm_mid, sm_post=sm_post, sm_pre=sm_pre,
                d_lb=d_lb, d_wgk=d_wgk, d_bgk=d_bgk)


MESH = pl.DeviceIdType.MESH
ANY = pl.BlockSpec(memory_space=pl.ANY)
N_REL = N_DEV - 1


def _place():
    return lax.axis_index("x"), lax.axis_index("y"), lax.axis_index("c")


def _slot(p):
    return 4 * p[0] + 2 * p[1] + p[2]


HBM = pl.BlockSpec(memory_space=pltpu.HBM)
SEM = pl.BlockSpec(memory_space=pltpu.SEMAPHORE)
EFFECT = pltpu.SideEffectType.DATAFLOW_SIDE_EFFECTING


def _peer_of(x, y, c, k):
    flip = lambda v, bit: 1 - v if bit else v
    return flip(x, k & 4), flip(y, k & 2), flip(c, k & 1)


def _view_whole(src, slot):
    return src


def _view_near(src, slot):
    return src


_view_near.peers = (1, 2, 4, 6)


def _view_near_rows(rows):
    def view(src, slot):
        return src.at[pl.ds(rows[0], rows[1] - rows[0])]
    view.peers = _view_near.peers
    view.land = lambda land, slot: land.at[slot, pl.ds(rows[0], rows[1] - rows[0])]
    return view


def _view_block(src, slot):
    return src.at[slot]


W_IN_SHARD = W_IN_REF // N_DEV


def _view_window(rows):
    def view(src, slot):
        col0 = pl.multiple_of((W_IN_SHARD * slot // DH) * DH, DH)
        return src.at[pl.ds(rows[0], rows[1] - rows[0]), pl.ds(col0, D)]
    return view


def _split_copies(view, srcs, lands, send_sems, recv_sems, local_sems):
    x, y, c = _place()
    me = _slot((x, y, c))
    into = getattr(view, "land", lambda land, slot: land.at[slot])
    local, sends, waits = [], [], []
    for a, (src, land) in enumerate(zip(srcs, lands)):
        local.append(pltpu.make_async_copy(view(src, me), into(land, me), local_sems.at[a]))
        for k in getattr(view, "peers", range(1, N_DEV)):
            peer = _peer_of(x, y, c, k)
            mine = view(src, _slot(peer))
            sems = dict(send_sem=send_sems.at[N_REL * a + k - 1], recv_sem=recv_sems.at[N_REL * a + k - 1],
                        device_id=peer, device_id_type=MESH)
            sends.append(pltpu.make_async_remote_copy(src_ref=mine, dst_ref=into(land, me), **sems))
            waits.append(pltpu.make_async_remote_copy(src_ref=mine, dst_ref=into(land, _slot(peer)), **sems))
    return local, sends, waits


def _split_start(groups, name, after):
    built = []
    for view, srcs, lands in groups:
        lands = [lax.empty(l, s.dtype) if isinstance(l, tuple) else l for l, s in zip(lands, srcs)]
        built.append((view, list(srcs), lands))
    bufs = [b for _, srcs, lands in built for b in srcs + lands]
    nb, ng = len(bufs), len(built)

    def body(*refs):
        buf_refs, sem_refs, token = refs[:nb], refs[nb + 1:nb + 1 + 3 * ng], refs[-1]
        pos = 0
        for i, (view, srcs, _) in enumerate(built):
            n = len(srcs)
            local, sends, _ = _split_copies(view, buf_refs[pos:pos + n], buf_refs[pos + n:pos + 2 * n],
                                            *sem_refs[3 * i:3 * i + 3])
            pos += 2 * n
            for cp in local + sends:
                cp.start()
        token[...] = jnp.zeros_like(token)

    sems = []
    for _, srcs, _ in built:
        n = len(srcs)
        sems += [pltpu.SemaphoreType.DMA((N_REL * n,)), pltpu.SemaphoreType.DMA((N_REL * n,)),
                 pltpu.SemaphoreType.DMA((n,))]
    hbm = lambda a: pltpu.with_memory_space_constraint(a, pltpu.HBM)
    out = pl.pallas_call(
        body, name=name,
        out_shape=(*sems, *[pltpu.HBM(b.shape, b.dtype) for b in bufs], jax.ShapeDtypeStruct((8, DH), F32)),
        in_specs=[HBM] * nb + [ANY],
        out_specs=(*([SEM] * (3 * ng)), *([HBM] * nb), pl.BlockSpec(memory_space=pltpu.VMEM)),
        input_output_aliases={i: 3 * ng + i for i in range(nb)},
        compiler_params=pltpu.CompilerParams(has_side_effects=EFFECT),
    )(*[hbm(b) for b in bufs], after)
    handles, pos = [], 3 * ng
    for i, (view, srcs, _) in enumerate(built):
        n = len(srcs)
        handles.append(dict(view=view, n=n, sems=out[3 * i:3 * i + 3], srcs=list(out[pos:pos + n]),
                            lands=list(out[pos + n:pos + 2 * n])))
        pos += 2 * n
    return handles, out[-1]


def _split_wait(handle, name, after, srcs=None, lands=None):
    view, n, sems = handle["view"], handle["n"], handle["sems"]
    srcs = handle["srcs"] if srcs is None else srcs
    lands = handle["lands"] if lands is None else lands
    afters = list(after) if isinstance(after, (list, tuple)) else [after]

    def body(*refs):
        src_refs, land_refs = refs[:n], refs[n:2 * n]
        send_sems, recv_sems, local_sems = refs[2 * n:2 * n + 3]
        local, _, waits = _split_copies(view, src_refs, land_refs, send_sems, recv_sems, local_sems)
        for cp in waits:
            cp.wait_send()
            cp.wait_recv()
        for cp in local:
            cp.wait()

    out = pl.pallas_call(
        body, name=name,
        out_shape=(*[pltpu.HBM(s.shape, s.dtype) for s in srcs], *[pltpu.HBM(l.shape, l.dtype) for l in lands]),
        in_specs=[HBM] * (2 * n) + [SEM, SEM, SEM] + [ANY] * len(afters),
        out_specs=tuple([HBM] * (2 * n)),
        input_output_aliases={i: i for i in range(2 * n)},
        compiler_params=pltpu.CompilerParams(has_side_effects=EFFECT),
    )(*srcs, *lands, *sems, *afters)
    handle["srcs"] = list(out[:n])
    return list(out[n:])


def _tie(x, token, name):
    def body(x_ref, t_ref, o_ref):
        pass

    return pl.pallas_call(
        body, name=name, out_shape=jax.ShapeDtypeStruct(x.shape, x.dtype),
        in_specs=[ANY, ANY], out_specs=ANY, input_output_aliases={0: 0},
    )(x, token)


def _forward_to_sibling(land, name):
    def body(land_ref, out_ref, send_sems, recv_sems):
        x, y, c = _place()
        sibling = (x, y, 1 - c)
        chips = [(1 - x, y), (x, 1 - y), (1 - x, 1 - y)]

        def copy(j, core):
            blk = _slot((*chips[j], core))
            return pltpu.make_async_remote_copy(src_ref=land_ref.at[blk], dst_ref=out_ref.at[blk],
                                                send_sem=send_sems.at[j], recv_sem=recv_sems.at[j],
                                                device_id=sibling, device_id_type=MESH)

        sends = [copy(j, c) for j in range(3)]
        for cp in sends:
            cp.start()
        for j in range(3):
            copy(j, 1 - c).wait_recv()
        for cp in sends:
            cp.wait_send()

    return pl.pallas_call(
        body, name=name, in_specs=[ANY], out_specs=ANY, input_output_aliases={0: 0},
        out_shape=jax.ShapeDtypeStruct(land.shape, land.dtype),
        scratch_shapes=[pltpu.SemaphoreType.DMA((3,)), pltpu.SemaphoreType.DMA((3,))],
    )(land)


def _mod_fwd(a, w, b):
    def body(a_ref, w_ref, b_ref, o_ref):
        o_ref[...] = _dot(_silu(a_ref[...]), w_ref[...], NN, precision=HI) + b_ref[...]

    return pl.pallas_call(
        body, name="mod_fwd", out_shape=jax.ShapeDtypeStruct((a.shape[0], w.shape[1]), F32),
        compiler_params=pltpu.CompilerParams(vmem_limit_bytes=VMEM_LIMIT),
    )(a, w, b)


def _mod_bwd(a, d, w):
    def body(a_ref, d_ref, w_ref, dw_ref, dc_ref):
        av = a_ref[...]
        dv = d_ref[...]
        dw_ref[...] = _dot(_silu(av), dv, TN, precision=HI)
        da = _dot(dv[0:8, :], w_ref[...], NT, precision=HI) * _dsilu(av[0:8, :])
        row = lax.broadcasted_iota(jnp.int32, da.shape, 0)
        dc_ref[...] = jnp.where(row == 0, da, 0.0)

    return pl.pallas_call(
        body, name="mod_bwd",
        out_shape=[jax.ShapeDtypeStruct(w.shape, F32), jax.ShapeDtypeStruct((8, w.shape[0]), F32)],
        compiler_params=pltpu.CompilerParams(vmem_limit_bytes=VMEM_LIMIT),
    )(a, d, w)


def _sum_devices(g):
    def body(g_ref, o_ref):
        acc = g_ref[0]
        for i in range(1, g.shape[0]):
            acc = acc + g_ref[i]
        o_ref[...] = acc

    return pl.pallas_call(body, name="sum_devices_%d" % g.shape[1],
                          out_shape=jax.ShapeDtypeStruct(g.shape[1:], F32))(g)


def _sum_windows(g, name):
    n, r, c = g.shape
    tr = 128

    def body(g_ref, o_ref):
        x, y, cc = _place()
        lane0 = (W_IN_SHARD * _slot((x, y, cc))) % DH
        acc = g_ref[0].astype(F32)
        for i in range(1, n):
            acc = acc + g_ref[i].astype(F32)
        o_ref[...] = pltpu.roll(acc, (c - lane0) % c, 1).T

    return pl.pallas_call(
        body, name=name, grid=(r // tr,),
        in_specs=[pl.BlockSpec((n, tr, c), lambda i: (0, i, 0))],
        out_specs=pl.BlockSpec((c, tr), lambda i: (0, i)),
        out_shape=jax.ShapeDtypeStruct((c, r), F32),
        compiler_params=_cp("parallel"),
    )(g)


def _adam_rows(r, c, n):
    budget = 6 * 1024 * 1024
    best = None
    for tr in range(16, r + 1, 16):
        if r % tr == 0 and tr * c * (2 * n + 28) <= budget:
            best = tr
    return best if best is not None else r


def _adamw(g, w, m, v, name):
    n, r, c = g.shape
    tr = _adam_rows(r, c, n)
    bc1 = 1.0 - ADAM_B1 ** ADAM_STEP
    bc2 = 1.0 - ADAM_B2 ** ADAM_STEP

    def body(g_ref, w_ref, m_ref, v_ref, go_ref, d_ref, mo_ref, vo_ref):
        grad = g_ref[0].astype(F32)
        for i in range(1, n):
            grad = grad + g_ref[i].astype(F32)
        go_ref[...] = grad
        m_new = ADAM_B1 * m_ref[...] + (1.0 - ADAM_B1) * grad
        v_new = ADAM_B2 * v_ref[...] + (1.0 - ADAM_B2) * (grad * grad)
        mo_ref[...] = m_new
        vo_ref[...] = v_new
        d_ref[...] = -ADAM_LR * ((m_new / bc1) / (jnp.sqrt(v_new / bc2) + ADAM_EPS) + ADAM_WD * w_ref[...])

    blk = pl.BlockSpec((tr, c), lambda i: (i, 0))
    out = jax.ShapeDtypeStruct((r, c), F32)
    return pl.pallas_call(
        body, name=name, grid=(r // tr,),
        in_specs=[pl.BlockSpec((n, tr, c), lambda i: (0, i, 0)), blk, blk, blk],
        out_specs=[blk] * 4, out_shape=[out] * 4,
        compiler_params=_cp("parallel"),
    )(g, w, m, v)


ADAM_ROWS3 = 168


def _adam_math(grad, w, m, v):
    bc1 = 1.0 - ADAM_B1 ** ADAM_STEP
    bc2 = 1.0 - ADAM_B2 ** ADAM_STEP
    m_new = ADAM_B1 * m + (1.0 - ADAM_B1) * grad
    v_new = ADAM_B2 * v + (1.0 - ADAM_B2) * (grad * grad)
    delta = -ADAM_LR * ((m_new / bc1) / (jnp.sqrt(v_new / bc2) + ADAM_EPS) + ADAM_WD * w)
    return delta, m_new, v_new


def _adamw_rows3(g, w3, m3, v3, name):
    r, _, c = w3.shape
    n = ADAM_ROWS3
    starts = list(range(0, r - n, n)) + [r - n]

    def body(g_hbm, w_hbm, m_hbm, v_hbm, go_hbm, d_hbm, mo_hbm, vo_hbm, gbuf, ibuf, obuf, in_sems, out_sems):
        def fetch(p):
            r0, slot = starts[p], p % 2
            g0 = (r0 // 8) * 8
            cps = [pltpu.make_async_copy(g_hbm.at[pl.ds(g0, n + 8)], gbuf.at[slot], in_sems.at[slot, 0])]
            cps += [pltpu.make_async_copy(h.at[pl.ds(r0, n), 0], ibuf.at[slot, k], in_sems.at[slot, 1 + k])
                    for k, h in enumerate((w_hbm, m_hbm, v_hbm))]
            for cp in cps:
                cp.start()
            return cps

        pending, outs = fetch(0), []
        for p, r0 in enumerate(starts):
            slot = p % 2
            nxt = fetch(p + 1) if p + 1 < len(starts) else []
            for cp in pending:
                cp.wait()
            grad = gbuf[slot, pl.ds(r0 - (r0 // 8) * 8, n), :]
            delta, m_new, v_new = _adam_math(grad, ibuf[slot, 0], ibuf[slot, 1], ibuf[slot, 2])
            for cp in outs:
                cp.wait()
            for k, val in enumerate((grad, delta, m_new, v_new)):
                obuf[slot, k] = val
            outs = [pltpu.make_async_copy(obuf.at[slot, k], h.at[pl.ds(r0, n), 0], out_sems.at[slot, k])
                    for k, h in enumerate((go_hbm, d_hbm, mo_hbm, vo_hbm))]
            for cp in outs:
                cp.start()
            pending = nxt
        for cp in outs:
            cp.wait()

    out = jax.ShapeDtypeStruct(w3.shape, F32)
    return pl.pallas_call(
        body, name=name, in_specs=[ANY] * 4, out_specs=[ANY] * 4, out_shape=[out] * 4,
        scratch_shapes=[pltpu.VMEM((2, n + 8, c), F32), pltpu.VMEM((2, 3, n, c), F32), pltpu.VMEM((2, 4, n, c), F32),
                        pltpu.SemaphoreType.DMA((2, 4)), pltpu.SemaphoreType.DMA((2, 4))],
        compiler_params=pltpu.CompilerParams(vmem_limit_bytes=VMEM_LIMIT),
    )(g, w3, m3, v3)


def kernel(x, c, ctx, c_ctx, w_mod, b_mod, norm_pre1, norm_post1, norm_pre2, norm_post2, w_in, hg_lb, hg_onorm, gla_w_gk, gla_b_gk, gla_onorm, w_br_hg, w_br_gla, w_out, w_ff_gate, w_ff_up, w_ff_down, loss_target, m_c_ctx, m_w_mod, m_b_mod, m_norm_pre1, m_norm_post1, m_norm_pre2, m_norm_post2, m_w_in, m_hg_lb, m_hg_onorm, m_gla_w_gk, m_gla_b_gk, m_gla_onorm, m_w_br_hg, m_w_br_gla, m_w_out, m_w_ff_gate, m_w_ff_up, m_w_ff_down, v_c_ctx, v_w_mod, v_b_mod, v_norm_pre1, v_norm_post1, v_norm_pre2, v_norm_post2, v_w_in, v_hg_lb, v_hg_onorm, v_gla_w_gk, v_gla_b_gk, v_gla_onorm, v_w_br_hg, v_w_br_gla, v_w_out, v_w_ff_gate, v_w_ff_up, v_w_ff_down):
    xi, yi, ci = lax.axis_index("x"), lax.axis_index("y"), lax.axis_index("c")
    me = 4 * xi + 2 * yi + ci
    t = CTX + x.shape[1]

    w_in_pieces, w_in_state = [], {}

    def w_in_piece(i):
        return (_view_near_rows((i * W_IN_PIECE, (i + 1) * W_IN_PIECE)), w_in_state["src"], w_in_state["land"])

    def started_w_in(handle):
        w_in_state.update(src=handle["srcs"], land=handle["lands"])
        w_in_pieces.append(handle)

    tr_ = lambda a: jnp.swapaxes(a[0], 0, 1)
    w_in_bf = jnp.pad(w_in[0].astype(BF16), ((0, 0), (0, W_IN_PAD - W_IN_SHARD)))
    w_in_state.update(src=[w_in_bf], land=[lax.empty((N_DEV,) + w_in_bf.shape, BF16)])
    gathered = lambda arrs: [(N_DEV,) + a.shape for a in arrs]
    whole = lambda arrs: (_view_whole, arrs, gathered(arrs))
    small_in = [c, hg_lb, gla_w_gk[0], gla_b_gk[0]]
    (small_handle, piece), tok = _split_start([whole(small_in), w_in_piece(0)], "ag_small_start", c)
    started_w_in(piece)
    c_all, lb_g, wgk_g, bgk_g = _split_wait(small_handle, "ag_small_wait", tok)
    big = [w_in[0], w_br_hg[0], w_br_gla[0], w_out[0], tr_(w_ff_gate), tr_(w_ff_up), w_ff_down[0]]
    big_bf = [None] + [w.astype(BF16) for w in big[1:]]
    cols = lambda g: jnp.transpose(g, (1, 0, 2)).reshape(g.shape[1], N_DEV * g.shape[2])

    def get_w_in(after):
        for i, handle in enumerate(w_in_pieces):
            land = _split_wait(handle, "ag_w_in_wait%d" % i, after, srcs=w_in_state["src"], lands=w_in_state["land"])
            w_in_state.update(src=handle["srcs"], land=land)
        return _assemble_w_in(_forward_to_sibling(w_in_state["land"][0], "ag_w_in_forward"))

    def get_mix(after):
        g_brh, g_brg, g_out = _split_wait(mix_handle, "ag_mix_wait", after)
        return _gate_cols(cols(g_brh)), _gate_cols(cols(g_brg)), _gate_rows(g_out.reshape(D, D))

    def get_ffn(after):
        g_gate, g_up, g_down = _split_wait(ffn_handle, "ag_ffn_wait", after)
        return (g_gate.reshape(D_FF, D), g_up.reshape(D_FF, D)), g_down.reshape(D_FF, D)

    hg_lb_full = jnp.transpose(lb_g, (1, 2, 0, 3)).reshape(2, 2, HW)
    wgk_k = _layout_wgk(jnp.transpose(wgk_g, (1, 2, 0, 3)).reshape(2, 16, HW)).astype(BF16)
    bgk_k = jnp.transpose(bgk_g, (1, 0, 2)).reshape(1, D)
    onw = jnp.concatenate([jnp.tile(hg_onorm, (1, NH // 2)), jnp.tile(gla_onorm, (1, NH // 2))], axis=1)

    n_mod = w_mod.shape[2]
    a9 = jnp.concatenate([c_ctx[None], c_all[:, 0], jnp.zeros((16 - 1 - N_DEV, D), F32)], axis=0)
    b_loc = lax.dynamic_slice(b_mod, (0, me * n_mod), (1, n_mod))
    s_loc = _mod_fwd(a9, w_mod[0], b_loc)
    (mod_handle, piece), tok = _split_start([whole([s_loc]), w_in_piece(1)], "ag_mod_start", s_loc)
    started_w_in(piece)
    s_all, = _split_wait(mod_handle, "ag_mod_wait", tok)
    mod_all = jnp.transpose(s_all, (1, 0, 2)).reshape(16, N_DEV * n_mod)
    pad8 = lambda m: jnp.concatenate([m.reshape(6, D), jnp.zeros((2, D), F32)], axis=0)
    modc = pad8(mod_all[0])
    modx = pad8(lax.dynamic_slice(mod_all, (1 + me, 0), (1, N_DEV * n_mod))[0])

    (mix_handle, ffn_handle), tok = _split_start([whole(big_bf[1:4]), whole(big_bf[4:])], "ag_big_start", s_all)

    z = (ctx[0], x[0])
    modx = _tie(modx, tok, "tie_mod")
    norms = (norm_pre1, norm_post1, norm_pre2, norm_post2)
    shard = lambda d: jnp.transpose(d.reshape(d.shape[0], N_DEV, -1), (1, 0, 2)).astype(BF16)
    rowshard = lambda d: d.reshape(N_DEV, d.shape[0] // N_DEV, d.shape[1]).astype(BF16)
    sent, w_in_grad = [], {}

    def w_in_chunk(i):
        half, rows = W_IN_GRAD_CHUNKS[i]
        return (_view_window(rows), w_in_grad[half], [(N_DEV, rows[1] - rows[0], D)])

    def sent_w_in(i, handle):
        w_in_grad[W_IN_GRAD_CHUNKS[i][0]] = handle["srcs"]
        sent.append(("w_in%d" % i, ["w_in#%d" % i], handle))

    def send(names, grads, x_after):
        if names == ("w_in_a",):
            w_in_grad["a"] = list(grads)
            (handle,), tok = _split_start([w_in_chunk(0)], "grads_w_in0_start", x_after)
            sent_w_in(0, handle)
            return _tie(x_after, tok, "tie_w_in0")
        if names == ("w_in_b",):
            w_in_grad["b"] = list(grads)
            return x_after
        arrs, leaves = [], []
        for nm, g in zip(names, grads):
            if nm in ("w_gate_t", "w_up_t"):
                arrs.append(rowshard(g))
                leaves.append({"w_gate_t": "w_ff_gate", "w_up_t": "w_ff_up"}[nm])
            elif nm == "w_down":
                arrs.append(rowshard(g))
                leaves.append("w_ff_down")
            elif nm == "w_out":
                arrs.append(rowshard(g[GOFF:GOFF + D]))
                leaves.append(nm)
            else:
                arrs.append(shard(g[:, GOFF:GOFF + D]))
                leaves.append(nm)
        (handle,), tok = _split_start([(_view_block, arrs, [a.shape for a in arrs])], "grads_%s_start" % names[0],
                                      x_after)
        sent.append((names[0], leaves, handle))
        return _tie(x_after, tok, "tie_" + names[0])

    r = _local_step(z, loss_target[0], modc, modx, norms, onw, hg_lb_full, wgk_k, bgk_k,
                    get_w_in, get_mix, get_ffn, send)
    grad_x = r["grad_x"][None]

    sm_pre, sm_mid, sm_fin = r["sm_pre"], r["sm_mid"], r["sm_final"]
    dmodc = jnp.stack([sm_pre[0], sm_pre[2], sm_mid[4], sm_mid[0], sm_mid[2], sm_fin[0]]).reshape(-1)
    dmodx = jnp.stack([sm_pre[1], sm_pre[3], sm_mid[5], sm_mid[1], sm_mid[3], sm_fin[1]]).reshape(-1)
    on = r["sm_post"][0].reshape(NH, DH)
    pieces = [dmodc, dmodx, sm_pre[4], sm_mid[7], sm_mid[6], sm_fin[2], on[:NH // 2].sum(0), on[NH // 2:].sum(0),
              r["d_lb"][:2].reshape(-1), _unlayout_wgk(r["d_wgk"]).reshape(-1), r["d_bgk"][0]]
    loss_local = (0.5 / D) * jnp.sum(r["loss_vec"])
    pieces.append(jnp.concatenate([loss_local.reshape(1), jnp.zeros((DH - 1,), F32)]))
    sizes = [p.shape[0] for p in pieces]
    pack = jnp.concatenate(pieces).reshape(-1, DH)
    moms = [(m_w_in, v_w_in), (m_w_br_hg, v_w_br_hg), (m_w_br_gla, v_w_br_gla), (m_w_out, v_w_out),
            (m_w_ff_gate, v_w_ff_gate), (m_w_ff_up, v_w_ff_up), (m_w_ff_down, v_w_ff_down)]
    names = ["w_in", "w_br_hg", "w_br_gla", "w_out", "w_ff_gate", "w_ff_up", "w_ff_down"]
    wmv = {nm: (w, m, v) for nm, w, (m, v) in zip(names, big, moms)}
    res = {}

    def update(nm):
        w, m, v = wmv[nm]
        if nm in ("w_ff_gate", "w_ff_up"):
            outs = _adamw(recv[nm], w, tr_(m), tr_(v), "adamw_" + nm)
            res[nm] = [jnp.swapaxes(o, 0, 1)[None] for o in outs]
        else:
            res[nm] = [o[None] for o in _adamw(recv[nm], w, m[0], v[0], "adamw_" + nm)]

    (small_handle, handle), tok = _split_start([whole([pack]), w_in_chunk(1)], "small_grads_start", pack)
    sent_w_in(1, handle)
    recv = {}
    for first, leaves, handle in sent:
        if not first.startswith("w_in"):
            recv.update(zip(leaves, _split_wait(handle, "grads_%s_wait" % first, tok)))
    update("w_ff_gate")
    update("w_ff_up")
    pack_all, = _split_wait(small_handle, "small_grads_wait", [res["w_ff_gate"][0], res["w_ff_up"][0]])
    tot = _sum_devices(pack_all).reshape(-1)
    offs = [sum(sizes[:i]) for i in range(len(sizes))]
    part = lambda i: tot[offs[i]:offs[i] + sizes[i]]
    dmodc_t, dmodx_t = part(0), part(1)
    g_b_mod = (dmodc_t + dmodx_t)[None]
    g_norms = [part(i)[None] for i in (2, 3, 4, 5)]
    g_hg_on, g_gla_on = part(6)[None], part(7)[None]
    lb0 = lax.dynamic_slice(part(8).reshape(2, HW), (0, me * (HW // N_DEV)), (2, HW // N_DEV))
    g_hg_lb = jnp.stack([lb0, -lb0])
    g_wgk = lax.dynamic_slice(part(9).reshape(2, 16, HW), (0, 0, me * (HW // N_DEV)), (2, 16, HW // N_DEV))[None]
    g_bgk = lax.dynamic_slice(part(10).reshape(2, HW), (0, me * (HW // N_DEV)), (2, HW // N_DEV))[None]
    loss = part(11)[0]

    dmx_all = pack_all.reshape(N_DEV, -1)[:, sizes[0]:sizes[0] + sizes[1]]
    d9 = jnp.concatenate([lax.dynamic_slice(dmodc_t[None], (0, me * n_mod), (1, n_mod)),
                          lax.dynamic_slice(dmx_all, (0, me * n_mod), (N_DEV, n_mod)),
                          jnp.zeros((16 - 1 - N_DEV, n_mod), F32)], axis=0)
    g_w_mod, dcc_part = _mod_bwd(a9, d9, w_mod[0])
    (cctx_handle, handle), tok = _split_start([whole([dcc_part]), w_in_chunk(2)], "c_ctx_start", dcc_part)
    sent_w_in(2, handle)
    recv["w_ff_down"] = _tie(recv["w_ff_down"], tok, "tie_down")
    update("w_ff_down")
    res["w_mod"] = [o[None] for o in _adamw(g_w_mod[None], w_mod[0], m_w_mod[0], v_w_mod[0], "adamw_w_mod")]
    for nm in ("w_out", "w_br_hg", "w_br_gla"):
        update(nm)
    dcc_all, = _split_wait(cctx_handle, "c_ctx_wait", [res["w_ff_down"][0], res["w_mod"][0]])
    g_c_ctx = _sum_devices(dcc_all)[0]

    small = [("c_ctx", c_ctx, m_c_ctx, v_c_ctx, g_c_ctx), ("b_mod", b_mod, m_b_mod, v_b_mod, g_b_mod),
             ("norm_pre1", norm_pre1, m_norm_pre1, v_norm_pre1, g_norms[0]),
             ("norm_post1", norm_post1, m_norm_post1, v_norm_post1, g_norms[1]),
             ("norm_pre2", norm_pre2, m_norm_pre2, v_norm_pre2, g_norms[2]),
             ("norm_post2", norm_post2, m_norm_post2, v_norm_post2, g_norms[3]),
             ("hg_lb", hg_lb, m_hg_lb, v_hg_lb, g_hg_lb), ("hg_onorm", hg_onorm, m_hg_onorm, v_hg_onorm, g_hg_on),
             ("gla_w_gk", gla_w_gk, m_gla_w_gk, v_gla_w_gk, g_wgk), ("gla_b_gk", gla_b_gk, m_gla_b_gk, v_gla_b_gk, g_bgk),
             ("gla_onorm", gla_onorm, m_gla_onorm, v_gla_onorm, g_gla_on)]
    flat = lambda k: jnp.concatenate([s[k].reshape(-1) for s in small]).reshape(-1, DH)
    outs = _adamw(flat(4)[None], flat(1), flat(2), flat(3), "adamw_small")
    off = 0
    for nm, w, _, _, _ in small:
        res[nm] = [o.reshape(-1)[off:off + w.size].reshape(w.shape) for o in outs]
        off += w.size

    done = [res[nm][0] for nm in names[1:]] + [res["w_mod"][0], outs[0]]
    sums = []
    for i, (first, leaves, handle) in enumerate(s for s in sent if s[0].startswith("w_in")):
        half = W_IN_GRAD_CHUNKS[i][0]
        land, = _split_wait(handle, "grads_%s_wait" % first, done, srcs=w_in_grad[half])
        w_in_grad[half] = handle["srcs"]
        sums.append(_sum_windows(land, "sum_windows%d" % i))
    major = lambda a: jnp.transpose(a, (2, 0, 1))
    outs = _adamw_rows3(jnp.concatenate(sums, axis=1), major(w_in), major(m_w_in), major(v_w_in), "adamw_w_in")
    res["w_in"] = [jnp.transpose(o, (1, 2, 0)) for o in outs]

    order = ["c_ctx", "w_mod", "b_mod", "norm_pre1", "norm_post1", "norm_pre2", "norm_post2", "w_in", "hg_lb",
             "hg_onorm", "gla_w_gk", "gla_b_gk", "gla_onorm", "w_br_hg", "w_br_gla", "w_out", "w_ff_gate", "w_ff_up",
             "w_ff_down"]
    return (loss, grad_x, *[res[n][k] for k in range(4) for n in order])
```

```python
import functools

import jax
import jax.numpy as jnp
from jax import lax
from jax.experimental import pallas as pl
from jax.experimental.pallas import tpu as pltpu

F32 = jnp.float32
BF16 = jnp.bfloat16
HI = lax.Precision.HIGHEST

N_DEV = 8
D = 1024
CTX = 256
HW = 512
DH = 128
NH = 8
D_FF = 2816
EPS = 1e-6
GLA_NORM = 16.0
CHUNK = 64
TR = 256
NCT = CTX // TR
W_IN_COLS = 7168
MAIN0 = 0
LR0 = 4608
GW = 1152
GOFF = 32
GATE_HG0 = LR0
GATE_GLA0 = LR0 + D
LEVELS = (32, 16, 8)
EXP_CLAMP = 80.0
VMEM_LIMIT = 48 * 1024 * 1024

ADAM_LR, ADAM_B1, ADAM_B2, ADAM_EPS, ADAM_WD, ADAM_STEP = 0.001, 0.9, 0.999, 1e-08, 0.01, 10


def _cp(*sem):
    return pltpu.CompilerParams(dimension_semantics=sem, vmem_limit_bytes=VMEM_LIMIT)


def _sig(x):
    return jax.nn.sigmoid(x)


def _silu(x):
    return x * _sig(x)


def _dsilu(x):
    s = _sig(x)
    return s * (1.0 + x * (1.0 - s))


def _rstd(x):
    return lax.rsqrt(jnp.mean(x * x, axis=-1, keepdims=True) + EPS)


def _rms_bwd(a, y, r):
    return r * (a - y * (r * r) * jnp.mean(a * y, axis=-1, keepdims=True))


def _colsum(x):
    return jnp.sum(x, axis=0, keepdims=True)


def _dot(a, b, dims, precision=None):
    return lax.dot_general(a, b, (dims, ((), ())), preferred_element_type=F32, precision=precision)


NN = ((1,), (0,))
NT = ((1,), (1,))
TN = ((0,), (0,))

SCAN_HEADS_FWD = 4
SCAN_HEADS_BWD = 4


def _split_dot(m, x):
    mb = m.astype(BF16)
    x1 = x.astype(BF16)
    r1 = x - x1.astype(F32)
    x2 = r1.astype(BF16)
    x3 = (r1 - x2.astype(F32)).astype(BF16)
    return _dot(mb, x1, NN) + _dot(mb, x2, NN) + _dot(mb, x3, NN)


def _matmul(a, b, dims, out_dtype, name, tm, tn, tk, a_off=0, m_out=None):
    a_pair = isinstance(a, (tuple, list))
    as_ = list(a) if a_pair else [a]
    a = as_[0]
    pair = isinstance(b, (tuple, list))
    bs = list(b) if pair else [b]
    b1 = bs[0]
    rows = b1.shape[0] * len(bs)
    half = None
    if dims == NN:
        m, k, n = a.shape[0], rows, b1.shape[1]
        a_spec = pl.BlockSpec((tm, tk), lambda i, j, kk: (i, kk + a_off))
        half = b1.shape[0] // tk
        if a_pair:
            assert pair and a.shape[1] == b1.shape[0] and a_off == 0
            a_spec = [pl.BlockSpec((tm, tk), lambda i, j, kk: (i, jnp.minimum(kk, half - 1))),
                      pl.BlockSpec((tm, tk), lambda i, j, kk: (i, jnp.maximum(kk - half, 0)))]
        b_maps = [lambda i, j, kk: (kk, j)] if not pair else [
            lambda i, j, kk: (jnp.minimum(kk, half - 1), j), lambda i, j, kk: (jnp.maximum(kk - half, 0), j)]
        b_specs = [pl.BlockSpec((tk, tn), f) for f in b_maps]
        axis = 2
    elif dims == NT:
        m, k, n = a.shape[0], b1.shape[1], rows
        a_spec = pl.BlockSpec((tm, tk), lambda i, j, kk: (i, kk + a_off))
        half = b1.shape[0] // tn
        b_maps = [lambda i, j, kk: (j, kk)] if not pair else [
            lambda i, j, kk: (jnp.minimum(j, half - 1), kk), lambda i, j, kk: (jnp.maximum(j - half, 0), kk)]
        b_specs = [pl.BlockSpec((tn, tk), f) for f in b_maps]
        axis = 1
    else:
        assert not pair
        m, k = (a.shape[1] if m_out is None else m_out), a.shape[0]
        n = b1.shape[1]
        a_spec = pl.BlockSpec((tk, tm), lambda i, j, kk: (kk, i + a_off))
        b_specs = [pl.BlockSpec((tk, tn), lambda i, j, kk: (kk, j))]
    assert m % tm == 0 and n % tn == 0 and k % tk == 0, (name, m, n, k, tm, tn, tk)
    nk = k // tk
    nb = len(bs)
    na = len(as_)
    assert na == 1 or dims == NN

    def body(*refs):
        a_refs, refs = refs[:na], refs[na:]
        o_ref = refs[nb]
        if pair:
            bv = jnp.where(pl.program_id(axis) < half, refs[0][...], refs[1][...])
        else:
            bv = refs[0][...]
        av = a_refs[0][...] if na == 1 else jnp.where(pl.program_id(2) < half, a_refs[0][...], a_refs[1][...])
        part = _dot(av, bv, dims)
        if nk == 1:
            o_ref[...] = part.astype(o_ref.dtype)
            return
        acc_ref = refs[nb + 1]
        kk = pl.program_id(2)

        @pl.when(kk == 0)
        def _():
            acc_ref[...] = part

        @pl.when(kk > 0)
        def _():
            acc_ref[...] += part

        @pl.when(kk == nk - 1)
        def _():
            o_ref[...] = acc_ref[...].astype(o_ref.dtype)

    return pl.pallas_call(
        body,
        name=name,
        grid=(m // tm, n // tn, nk),
        in_specs=(a_spec if a_pair else [a_spec]) + b_specs,
        out_specs=pl.BlockSpec((tm, tn), lambda i, j, kk: (i, j)),
        out_shape=jax.ShapeDtypeStruct((m, n), out_dtype),
        scratch_shapes=[] if nk == 1 else [pltpu.VMEM((tm, tn), F32)],
        compiler_params=_cp("parallel", "parallel", "arbitrary"),
    )(*as_, *bs)


def _mm_gu_act(h, w_gate_t, w_up_t, name, tm):
    t = h.shape[0]
    tn = D_FF // 2

    def body(a_ref, bg_ref, bu_ref, u_ref, v_ref, act_ref):
        a = a_ref[...]
        u = _dot(a, bg_ref[...], NT)
        v = _dot(a, bu_ref[...], NT)
        u_ref[...] = u.astype(BF16)
        v_ref[...] = v.astype(BF16)
        act_ref[...] = (_silu(u) * v).astype(BF16)

    wspec = pl.BlockSpec((tn, D), lambda i, j: (j, 0))
    ospec = pl.BlockSpec((tm, tn), lambda i, j: (i, j))
    out = jax.ShapeDtypeStruct((t, D_FF), BF16)
    return pl.pallas_call(
        body, name=name, grid=(t // tm, D_FF // tn),
        in_specs=[pl.BlockSpec((tm, D), lambda i, j: (i, 0)), wspec, wspec],
        out_specs=[ospec] * 3, out_shape=[out] * 3,
        compiler_params=_cp("parallel", "parallel"),
    )(h, w_gate_t, w_up_t)


def _mm_down_dx_act(dy, w_down, u, v, name, tm):
    t = dy.shape[0]
    tn = D_FF // 2

    def body(a_ref, b_ref, u_ref, v_ref, du_ref, dv_ref):
        dact = _dot(a_ref[...], b_ref[...], NT)
        u = u_ref[...].astype(F32)
        du_ref[...] = (dact * v_ref[...].astype(F32) * _dsilu(u)).astype(BF16)
        dv_ref[...] = (dact * _silu(u)).astype(BF16)

    ospec = pl.BlockSpec((tm, tn), lambda i, j: (i, j))
    out = jax.ShapeDtypeStruct((t, D_FF), BF16)
    return pl.pallas_call(
        body, name=name, grid=(t // tm, D_FF // tn),
        in_specs=[pl.BlockSpec((tm, D), lambda i, j: (i, 0)), pl.BlockSpec((tn, D), lambda i, j: (j, 0)), ospec, ospec],
        out_specs=[ospec] * 2, out_shape=[out] * 2,
        compiler_params=_cp("parallel", "parallel"),
    )(dy, w_down, u, v)


def _row(c):
    return pl.BlockSpec((TR, c), lambda i: (i, 0))


def _rowcol(width, cb):
    return pl.BlockSpec((TR, width), lambda i: (i, cb))


def _full(shape):
    return pl.BlockSpec(shape, lambda i: (0,) * len(shape))


def _mod_row(mc_ref, mx_ref, k, is_ctx):
    return jnp.where(is_ctx, mc_ref[k:k + 1, :], mx_ref[k:k + 1, :])


def _z_specs():
    return [pl.BlockSpec((TR, D), lambda i: (jnp.minimum(i, NCT - 1), 0)),
            pl.BlockSpec((TR, D), lambda i: (jnp.maximum(i - NCT, 0), 0))]


def _z_tile(c_ref, x_ref, is_ctx):
    return jnp.where(is_ctx, c_ref[...], x_ref[...])


def _acc_row(ref, k, val):
    ref[k:k + 1, :] += val


def _acc_mod(ref, k, is_ctx, val):
    zero = jnp.zeros_like(val)
    ref[k:k + 1, :] += jnp.where(is_ctx, val, zero)
    ref[k + 1:k + 2, :] += jnp.where(is_ctx, zero, val)


def _prenorm(z, nw, modc, modx, i_shift, i_scale, name):
    t = z[0].shape[0] + z[1].shape[0]

    def body(zc_ref, zx_ref, nw_ref, mc_ref, mx_ref, h_ref):
        is_ctx = pl.program_id(0) < NCT
        x = _z_tile(zc_ref, zx_ref, is_ctx)
        n = x * _rstd(x) * nw_ref[...]
        h = n * (1.0 + _mod_row(mc_ref, mx_ref, i_scale, is_ctx)) + _mod_row(mc_ref, mx_ref, i_shift, is_ctx)
        h_ref[...] = h.astype(BF16)

    return pl.pallas_call(
        body, name=name, grid=(t // TR,),
        in_specs=_z_specs() + [_full((1, D)), _full((8, D)), _full((8, D))],
        out_specs=_row(D),
        out_shape=jax.ShapeDtypeStruct((t, D), BF16),
        compiler_params=_cp("parallel"),
    )(*z, nw, modc, modx)


def _hg_lb(lb_ref, d):
    a0 = lb_ref[0, d:d + 1, :]
    a1 = lb_ref[1, d:d + 1, :]
    mx = jnp.maximum(a0, a1)
    e0 = jnp.exp(a0 - mx)
    e1 = jnp.exp(a1 - mx)
    return e0 / (e0 + e1)


def _log_sigmoid(x):
    return jnp.minimum(x, 0.0) - jnp.log(1.0 + jnp.exp(-jnp.abs(x)))


def _gates_fwd(p, hg_lb, wgk, bgk):
    t = p.shape[0]
    seg = lambda j: _rowcol(HW, MAIN0 // HW + j)

    def body(hq_ref, hi_ref, hf_ref, hb_ref, gq_ref, gk_ref, gv_ref, lr_ref, lb_ref, wgk_ref, bgk_ref,
             q_ref, v_ref, kf_ref, kb_ref, gf_ref, gb_ref):
        q_ref[:, :HW] = _silu(hq_ref[...].astype(F32)).astype(BF16)
        q_ref[:, HW:] = (gq_ref[...].astype(F32) * (DH ** -0.5)).astype(BF16)
        v_ref[:, :HW] = hi_ref[...]
        v_ref[:, HW:] = gv_ref[...]
        xg = _dot(lr_ref[...].astype(BF16), wgk_ref[...], NN) + bgk_ref[...]
        for d, (raw_ref, k_ref, g_ref) in enumerate(((hf_ref, kf_ref, gf_ref), (hb_ref, kb_ref, gb_ref))):
            lbd = _hg_lb(lb_ref, d)
            f = lbd + (1.0 - lbd) * _sig(raw_ref[...].astype(F32))
            k_ref[:, :HW] = (1.0 - f).astype(BF16)
            k_ref[:, HW:] = gk_ref[...]
            g_ref[:, :HW] = jnp.log(f)
            g_ref[:, HW:] = _log_sigmoid(xg[:, d * HW:(d + 1) * HW]) * (1.0 / GLA_NORM)

    out = jax.ShapeDtypeStruct((t, D), F32)
    outb = jax.ShapeDtypeStruct((t, D), BF16)
    return pl.pallas_call(
        body, name="gates_fwd", grid=(t // TR,),
        in_specs=[seg(0), seg(1), seg(2), seg(3), seg(5), seg(6), seg(7), _rowcol(DH, LR0 // DH),
                  _full((2, 2, HW)), _full((DH, D)), _full((1, D))],
        out_specs=[_row(D)] * 6,
        out_shape=[outb] * 4 + [out] * 2,
        compiler_params=_cp("parallel"),
    )(p, p, p, p, p, p, p, p, hg_lb, wgk, bgk)


def _post_fwd(o_fw, o_bw, p, onw):
    t = o_fw.shape[0]

    def body(of_ref, ob_ref, g1_ref, g2_ref, w_ref, y_ref):
        for h in range(NH):
            sl = slice(h * DH, (h + 1) * DH)
            o = of_ref[:, sl] + ob_ref[:, sl]
            g_ref = g1_ref if h < NH // 2 else g2_ref
            gs = slice((h % (NH // 2)) * DH, (h % (NH // 2) + 1) * DH)
            n = o * _rstd(o) * w_ref[:, sl]
            y_ref[:, sl] = (n * _silu(g_ref[:, gs].astype(F32))).astype(BF16)

    return pl.pallas_call(
        body, name="post_fwd", grid=(t // TR,),
        in_specs=[_row(D), _row(D), _rowcol(HW, MAIN0 // HW + 4), _rowcol(HW, MAIN0 // HW + 8), _full((1, D))],
        out_specs=_row(D),
        out_shape=jax.ShapeDtypeStruct((t, D), BF16),
        compiler_params=_cp("parallel"),
    )(o_fw, o_bw, p, p, onw)


def _gate_window_specs(col0):
    return [_rowcol(HW, col0 // HW), _rowcol(HW, col0 // HW + 1), _rowcol(DH, (col0 + 2 * HW) // DH)]


def _gate_window(refs):
    return jnp.concatenate([r[...].astype(F32) for r in refs], axis=1)


def _branch_merge(y, w_hg, w_gla, p):
    t = y.shape[0]

    def body(y_ref, wh_ref, wg_ref, a0, a1, a2, b0, b1, b2, u1_ref, u2_ref, m_ref):
        u1 = _dot(y_ref[:, :HW], wh_ref[...], NN)
        u2 = _dot(y_ref[:, HW:], wg_ref[...], NN)
        u1_ref[...] = u1.astype(BF16)
        u2_ref[...] = u2.astype(BF16)
        m_ref[...] = (_sig(_gate_window((a0, a1, a2))) * u1 + _sig(_gate_window((b0, b1, b2))) * u2).astype(BF16)

    out = jax.ShapeDtypeStruct((t, GW), BF16)
    return pl.pallas_call(
        body, name="branch_merge", grid=(t // TR,),
        in_specs=[_row(D), _full((HW, GW)), _full((HW, GW))] + _gate_window_specs(GATE_HG0)
        + _gate_window_specs(GATE_GLA0),
        out_specs=[_row(GW)] * 3, out_shape=[out] * 3,
        compiler_params=_cp("parallel"),
    )(y, w_hg, w_gla, p, p, p, p, p, p)


def _mid_fwd(z, y1, nw_post, nw_pre, modc, modx):
    t = y1.shape[0]

    def body(zc_ref, zx_ref, y_ref, wpo_ref, wpr_ref, mc_ref, mx_ref, z1_ref, h_ref):
        is_ctx = pl.program_id(0) < NCT
        y = y_ref[...].astype(F32)
        z1 = _z_tile(zc_ref, zx_ref, is_ctx) + _mod_row(mc_ref, mx_ref, 2, is_ctx) * (y * _rstd(y) * wpo_ref[...])
        z1_ref[...] = z1
        n = z1 * _rstd(z1) * wpr_ref[...]
        h = n * (1.0 + _mod_row(mc_ref, mx_ref, 4, is_ctx)) + _mod_row(mc_ref, mx_ref, 3, is_ctx)
        h_ref[...] = h.astype(BF16)

    return pl.pallas_call(
        body, name="mid_fwd", grid=(t // TR,),
        in_specs=_z_specs() + [_row(D), _full((1, D)), _full((1, D)), _full((8, D)), _full((8, D))],
        out_specs=[_row(D), _row(D)],
        out_shape=[jax.ShapeDtypeStruct((t, D), F32), jax.ShapeDtypeStruct((t, D), BF16)],
        compiler_params=_cp("parallel"),
    )(*z, y1, nw_post, nw_pre, modc, modx)


def _final(z1, y2, target, nw, modc, modx):
    t = z1.shape[0]

    def body(z1_ref, y_ref, tg_ref, w_ref, mc_ref, mx_ref, dz_ref, dy_ref, loss_ref, sm_ref):
        i = pl.program_id(0)
        is_ctx = i < NCT

        @pl.when(i == 0)
        def _():
            loss_ref[...] = jnp.zeros_like(loss_ref)
            sm_ref[...] = jnp.zeros_like(sm_ref)

        g = _mod_row(mc_ref, mx_ref, 5, is_ctx)
        y = y_ref[...].astype(F32)
        r = _rstd(y)
        w = w_ref[...]
        yr = y * r
        n = yr * w
        e = z1_ref[...] + g * n - tg_ref[...]
        lat = jnp.where(is_ctx, 0.0, 1.0)
        loss_ref[...] += lat * _colsum(e * e)
        dz = e * (lat / D)
        dz_ref[...] = dz
        _acc_mod(sm_ref, 0, is_ctx, _colsum(dz * n))
        dn = dz * g
        _acc_row(sm_ref, 2, _colsum(dn * yr))
        dy_ref[...] = _rms_bwd(dn * w, y, r).astype(BF16)

    return pl.pallas_call(
        body, name="final", grid=(t // TR,),
        in_specs=[_row(D), _row(D), pl.BlockSpec((TR, D), lambda i: (jnp.maximum(i - NCT, 0), 0)),
                  _full((1, D)), _full((8, D)), _full((8, D))],
        out_specs=[_row(D), _row(D), _full((1, D)), _full((8, D))],
        out_shape=[jax.ShapeDtypeStruct((t, D), F32), jax.ShapeDtypeStruct((t, D), BF16),
                   jax.ShapeDtypeStruct((1, D), F32), jax.ShapeDtypeStruct((8, D), F32)],
        compiler_params=_cp("arbitrary"),
    )(z1, y2, target, nw, modc, modx)


def _mid_bwd(dh2, dz, z1, y1, nw_post, nw_pre, modc, modx):
    t = z1.shape[0]

    def body(dh_ref, dz_ref, z1_ref, y_ref, wpo_ref, wpr_ref, mc_ref, mx_ref, dzo_ref, dy_ref, sm_ref):
        i = pl.program_id(0)
        is_ctx = i < NCT

        @pl.when(i == 0)
        def _():
            sm_ref[...] = jnp.zeros_like(sm_ref)

        dh = dh_ref[...].astype(F32)
        z1 = z1_ref[...]
        r = _rstd(z1)
        zr = z1 * r
        wpr = wpr_ref[...]
        n = zr * wpr
        _acc_mod(sm_ref, 0, is_ctx, _colsum(dh))
        _acc_mod(sm_ref, 2, is_ctx, _colsum(dh * n))
        dn = dh * (1.0 + _mod_row(mc_ref, mx_ref, 4, is_ctx))
        _acc_row(sm_ref, 6, _colsum(dn * zr))
        dz1 = dz_ref[...] + _rms_bwd(dn * wpr, z1, r)
        dzo_ref[...] = dz1
        y = y_ref[...].astype(F32)
        r1 = _rstd(y)
        yr = y * r1
        wpo = wpo_ref[...]
        g = _mod_row(mc_ref, mx_ref, 2, is_ctx)
        _acc_mod(sm_ref, 4, is_ctx, _colsum(dz1 * (yr * wpo)))
        dn1 = dz1 * g
        _acc_row(sm_ref, 7, _colsum(dn1 * yr))
        dy_ref[...] = _rms_bwd(dn1 * wpo, y, r1).astype(BF16)

    return pl.pallas_call(
        body, name="mid_bwd", grid=(t // TR,),
        in_specs=[_row(D)] * 4 + [_full((1, D)), _full((1, D)), _full((8, D)), _full((8, D))],
        out_specs=[_row(D), _row(D), _full((8, D))],
        out_shape=[jax.ShapeDtypeStruct((t, D), F32), jax.ShapeDtypeStruct((t, D), BF16),
                   jax.ShapeDtypeStruct((8, D), F32)],
        compiler_params=_cp("arbitrary"),
    )(dh2, dz, z1, y1, nw_post, nw_pre, modc, modx)


def _pre_bwd(dh1, dz, z, nw, modc, modx):
    t = dh1.shape[0]

    def body(dh_ref, dz_ref, zc_ref, zx_ref, w_ref, mc_ref, mx_ref, dzo_ref, sm_ref):
        i = pl.program_id(0)
        is_ctx = i < NCT

        @pl.when(i == 0)
        def _():
            sm_ref[...] = jnp.zeros_like(sm_ref)

        dh = dh_ref[...].astype(F32)
        x = _z_tile(zc_ref, zx_ref, is_ctx)
        r = _rstd(x)
        xr = x * r
        w = w_ref[...]
        _acc_mod(sm_ref, 0, is_ctx, _colsum(dh))
        _acc_mod(sm_ref, 2, is_ctx, _colsum(dh * (xr * w)))
        dn = dh * (1.0 + _mod_row(mc_ref, mx_ref, 1, is_ctx))
        _acc_row(sm_ref, 4, _colsum(dn * xr))
        dzo_ref[...] = dz_ref[...] + _rms_bwd(dn * w, x, r)

    return pl.pallas_call(
        body, name="pre_bwd", grid=(t // TR,),
        in_specs=[_row(D)] * 2 + _z_specs() + [_full((1, D)), _full((8, D)), _full((8, D))],
        out_specs=[pl.BlockSpec((TR, D), lambda i: (jnp.maximum(i - NCT, 0), 0)), _full((8, D))],
        out_shape=[jax.ShapeDtypeStruct((t - CTX, D), F32), jax.ShapeDtypeStruct((8, D), F32)],
        compiler_params=_cp("arbitrary"),
    )(dh1, dz, *z, nw, modc, modx)


def _branch_merge_bwd(dm, p, u1, u2, w_hg, w_gla):
    t = dm.shape[0]

    def body(dm_ref, a0, a1, a2, b0, b1, b2, u1_ref, u2_ref, wh_ref, wg_ref, du1_ref, du2_ref, dg_ref, dyh_ref, dyg_ref):
        dm_ = dm_ref[...].astype(F32)
        s1 = _sig(_gate_window((a0, a1, a2)))
        s2 = _sig(_gate_window((b0, b1, b2)))
        du1 = (dm_ * s1).astype(BF16)
        du2 = (dm_ * s2).astype(BF16)
        du1_ref[...] = du1
        du2_ref[...] = du2
        dg_ref[:, :GW] = (dm_ * u1_ref[...].astype(F32) * s1 * (1.0 - s1)).astype(BF16)
        dg_ref[:, GW:] = (dm_ * u2_ref[...].astype(F32) * s2 * (1.0 - s2)).astype(BF16)
        dyh_ref[...] = _dot(du1, wh_ref[...], NT).astype(BF16)
        dyg_ref[...] = _dot(du2, wg_ref[...], NT).astype(BF16)

    return pl.pallas_call(
        body, name="branch_merge_bwd", grid=(t // TR,),
        in_specs=[_row(GW)] + _gate_window_specs(GATE_HG0) + _gate_window_specs(GATE_GLA0)
        + [_row(GW), _row(GW), _full((HW, GW)), _full((HW, GW))],
        out_specs=[_row(GW), _row(GW), _row(2 * GW), _row(HW), _row(HW)],
        out_shape=[jax.ShapeDtypeStruct((t, GW), BF16), jax.ShapeDtypeStruct((t, GW), BF16),
                   jax.ShapeDtypeStruct((t, 2 * GW), BF16), jax.ShapeDtypeStruct((t, HW), BF16),
                   jax.ShapeDtypeStruct((t, HW), BF16)],
        compiler_params=_cp("parallel"),
    )(dm, p, p, p, p, p, p, u1, u2, w_hg, w_gla)


def _post_bwd(dy_hg, dy_gla, o_fw, o_bw, p, onw):
    t = o_fw.shape[0]

    def body(d1_ref, d2_ref, of_ref, ob_ref, g1_ref, g2_ref, w_ref, do_ref, dg_ref, sm_ref):
        @pl.when(pl.program_id(0) == 0)
        def _():
            sm_ref[...] = jnp.zeros_like(sm_ref)

        for h in range(NH):
            sl = slice(h * DH, (h + 1) * DH)
            gs = slice((h % (NH // 2)) * DH, (h % (NH // 2) + 1) * DH)
            g_ref, d_ref = (g1_ref, d1_ref) if h < NH // 2 else (g2_ref, d2_ref)
            o = of_ref[:, sl] + ob_ref[:, sl]
            r = _rstd(o)
            orr = o * r
            w = w_ref[:, sl]
            gt = g_ref[:, gs].astype(F32)
            dy = d_ref[:, gs].astype(F32)
            dg_ref[:, sl] = (dy * (orr * w) * _dsilu(gt)).astype(BF16)
            dn = dy * _silu(gt)
            sm_ref[0:1, sl] += _colsum(dn * orr)
            do_ref[:, sl] = _rms_bwd(dn * w, o, r)

    return pl.pallas_call(
        body, name="post_bwd", grid=(t // TR,),
        in_specs=[_row(HW), _row(HW), _row(D), _row(D), _rowcol(HW, MAIN0 // HW + 4), _rowcol(HW, MAIN0 // HW + 8),
                  _full((1, D))],
        out_specs=[_row(D), _row(D), _full((8, D))],
        out_shape=[jax.ShapeDtypeStruct((t, D), F32), jax.ShapeDtypeStruct((t, D), BF16),
                   jax.ShapeDtypeStruct((8, D), F32)],
        compiler_params=_cp("arbitrary"),
    )(dy_hg, dy_gla, o_fw, o_bw, p, p, onw)


def _gates_bwd(p, hg_lb, wgk, bgk, dgm, dgo, dq_f, dq_b, dv_f, dv_b, dk_f, dk_b, dg_f, dg_b):
    t = p.shape[0]
    seg = lambda j: _rowcol(HW, MAIN0 // HW + j)

    def body(hq_ref, hf_ref, hb_ref, lr_ref, lb_ref, wgk_ref, bgk_ref, dgm_ref, dgo_ref,
             dqf_ref, dqb_ref, dvf_ref, dvb_ref, dkf_ref, dkb_ref, dgf_ref, dgb_ref,
             dp_ref, dlb_ref, dw_ref, db_ref):
        @pl.when(pl.program_id(0) == 0)
        def _():
            dlb_ref[...] = jnp.zeros_like(dlb_ref)
            dw_ref[...] = jnp.zeros_like(dw_ref)
            db_ref[...] = jnp.zeros_like(db_ref)

        c0 = MAIN0

        def put(j, val):
            dp_ref[:, c0 + j * HW:c0 + (j + 1) * HW] = val.astype(BF16)

        dq = dqf_ref[...].astype(F32) + dqb_ref[...].astype(F32)
        dv = dvf_ref[...].astype(F32) + dvb_ref[...].astype(F32)
        put(0, dq[:, :HW] * _dsilu(hq_ref[...].astype(F32)))
        put(1, dv[:, :HW])
        put(5, dq[:, HW:] * (DH ** -0.5))
        put(7, dv[:, HW:])
        put(6, dkf_ref[:, HW:].astype(F32) + dkb_ref[:, HW:].astype(F32))
        dp_ref[:, c0 + 4 * HW:c0 + 5 * HW] = dgo_ref[:, :HW]
        dp_ref[:, c0 + 8 * HW:c0 + 9 * HW] = dgo_ref[:, HW:]
        lr = lr_ref[...].astype(BF16)
        xg = _dot(lr, wgk_ref[...], NN) + bgk_ref[...]
        dxg = []
        for d, (raw_ref, dk_ref, dg_ref) in enumerate(((hf_ref, dkf_ref, dgf_ref), (hb_ref, dkb_ref, dgb_ref))):
            lbd = _hg_lb(lb_ref, d)
            s = _sig(raw_ref[...].astype(F32))
            f = lbd + (1.0 - lbd) * s
            df = dg_ref[:, :HW] / f - dk_ref[:, :HW].astype(F32)
            put(2 + d, df * (1.0 - lbd) * s * (1.0 - s))
            dlb_ref[d:d + 1, :] += _colsum(df * (1.0 - s)) * (lbd * (1.0 - lbd))
            dxg.append(dg_ref[:, HW:] * (1.0 / GLA_NORM) * _sig(-xg[:, d * HW:(d + 1) * HW]))
        dxg = jnp.concatenate(dxg, axis=1)
        db_ref[0:1, :] += _colsum(dxg)
        dxg_b = dxg.astype(BF16)
        dw_ref[...] += _dot(lr, dxg_b, TN)
        dlr = _dot(dxg_b, wgk_ref[...], NT)
        dp_ref[:, LR0:LR0 + DH] = (dlr + dgm_ref[:, :DH].astype(F32)).astype(BF16)
        dp_ref[:, LR0 + DH:GATE_GLA0] = dgm_ref[:, DH:D]
        dp_ref[:, GATE_GLA0:GATE_GLA0 + DH] = dgm_ref[:, D:GW] + dgm_ref[:, GW:GW + DH]
        dp_ref[:, GATE_GLA0 + DH:GATE_GLA0 + GW] = dgm_ref[:, GW + DH:]
        dp_ref[:, GATE_GLA0 + GW:] = jnp.zeros((TR, W_IN_COLS - GATE_GLA0 - GW), BF16)

    return pl.pallas_call(
        body, name="gates_bwd", grid=(t // TR,),
        in_specs=[seg(0), seg(2), seg(3), _rowcol(DH, LR0 // DH), _full((2, 2, HW)), _full((DH, D)), _full((1, D)),
                  _row(2 * GW), _row(D)] + [_row(D)] * 8,
        out_specs=[_row(W_IN_COLS), _full((8, HW)), _full((DH, D)), _full((8, D))],
        out_shape=[jax.ShapeDtypeStruct((t, W_IN_COLS), BF16), jax.ShapeDtypeStruct((8, HW), F32),
                   jax.ShapeDtypeStruct((DH, D), F32), jax.ShapeDtypeStruct((8, D), F32)],
        compiler_params=_cp("arbitrary"),
    )(p, p, p, p, hg_lb, wgk, bgk, dgm, dgo, dq_f, dq_b, dv_f, dv_b, dk_f, dk_b, dg_f, dg_b)


def _scan_consts(rev):
    r = lax.broadcasted_iota(jnp.int32, (CHUNK, CHUNK), 0)
    u = lax.broadcasted_iota(jnp.int32, (CHUNK, CHUNK), 1)
    rp = lax.broadcasted_iota(jnp.int32, (CHUNK, 1), 0)
    if rev:
        r, u, rp = CHUNK - 1 - r, CHUNK - 1 - u, CHUNK - 1 - rp
    tri = jnp.where(u <= r, 1.0, 0.0).astype(F32)
    tri_t = jnp.where(r <= u, 1.0, 0.0).astype(F32)
    lv = []
    for b in LEVELS:
        sh = b.bit_length() - 1
        pair = ((r >> sh) == (u >> sh) + 1) & (((u >> sh) & 1) == 0)
        pair_t = ((u >> sh) == (r >> sh) + 1) & (((r >> sh) & 1) == 0)
        tside = ((rp >> sh) & 1) == 1
        lv.append((pair, pair_t, tside, jnp.where(tside, 1.0, -1.0).astype(F32)))
    bd = LEVELS[-1].bit_length() - 1
    diag = ((r >> bd) == (u >> bd)) & (u <= r)
    diag_t = ((r >> bd) == (u >> bd)) & (r <= u)
    return tri, tri_t, lv, diag, diag_t


def _row_of(pos, rev):
    return CHUNK - 1 - pos if rev else pos


def _chunk_terms(cum, b_scr, consts, rev):
    _, _, lv, _, _ = consts
    terms = []
    for b, (_, _, _, sgn) in zip(LEVELS, lv):
        pieces = []
        for j in range(CHUNK // (2 * b)):
            row = _row_of(2 * b * j + b - 1, rev)
            pieces.append(jnp.broadcast_to(b_scr[row:row + 1, :], (2 * b, DH)))
        if rev:
            pieces = pieces[::-1]
        bnd = pieces[0] if len(pieces) == 1 else jnp.concatenate(pieces, axis=0)
        terms.append(jnp.exp((cum - bnd) * sgn))
    b = LEVELS[-1]
    pieces = []
    for j in range(CHUNK // b):
        if j == 0:
            pieces.append(jnp.zeros((b, DH), F32))
        else:
            row = _row_of(b * j - 1, rev)
            pieces.append(jnp.broadcast_to(b_scr[row:row + 1, :], (b, DH)))
    if rev:
        pieces = pieces[::-1]
    start = jnp.concatenate(pieces, axis=0)
    wq = jnp.exp(jnp.minimum(cum - start, 0.0))
    wk = jnp.exp(jnp.minimum(start - cum, EXP_CLAMP))
    terms.append((wq, wk))
    return terms


def _run_staged(units):
    live = list(units)
    while live:
        nxt = []
        for u in live:
            try:
                next(u)
                nxt.append(u)
            except StopIteration:
                pass
        live = nxt


SCAN_TB = 256
SCAN_CB = SCAN_TB // CHUNK


def _block_order(i, ntb, rev):
    nctx = CTX // SCAN_TB
    if not rev:
        return i
    return jnp.where(i < nctx, nctx - 1 - i, ntb - 1 - (i - nctx))


def _chunk_in_block(j, rev):
    return SCAN_CB - 1 - j if rev else j


def _scan_fwd(q, k, v, g, rev):
    t = q.shape[0]
    nc = t // CHUNK
    hpb = SCAN_HEADS_FWD

    def body(q_ref, k_ref, v_ref, g_ref, o_ref, st_ref, s_scr, b_scr):
        consts = _scan_consts(rev)
        _, _, lv, diag, _ = consts
        masks = [lvl[0] for lvl in lv] + [diag]

        @pl.when(pl.program_id(1) == 0)
        def _():
            s_scr[...] = jnp.zeros_like(s_scr)

        tri = consts[0]
        state = {hh: s_scr[hh] for hh in range(hpb)}

        def unit(hh, j):
            sl = slice(hh * DH, (hh + 1) * DH)
            c = _chunk_in_block(j, rev)
            rows = slice(c * CHUNK, (c + 1) * CHUNK)
            b_ref = b_scr.at[hh * SCAN_CB + j]
            qc, kc, vc, gc = q_ref[rows, sl], k_ref[rows, sl], v_ref[rows, sl], g_ref[rows, sl]
            cum = _split_dot(tri, gc)
            b_ref[...] = cum
            yield
            terms = _chunk_terms(cum, b_ref, consts, rev)
            qf, kf = qc.astype(F32), kc.astype(F32)
            xs = [(jnp.where(tside, qf, kf) * w).astype(BF16) for w, (_, _, tside, _) in zip(terms[:-1], lv)]
            qd, kd = (qf * terms[-1][0]).astype(BF16), (kf * terms[-1][1]).astype(BF16)
            tot = _colsum(gc)
            qe = (qf * jnp.exp(cum)).astype(BF16)
            ke = (kf * jnp.exp(tot - cum)).astype(BF16)
            vb = vc.astype(BF16)
            yield
            scs = [_dot(x, x, NT) for x in xs] + [_dot(qd, kd, NT)]
            kv = _dot(vb, ke, TN)
            yield
            a = jnp.zeros((CHUNK, CHUNK), F32)
            for sc, m in zip(scs, masks):
                a = a + jnp.where(m, sc, 0.0)
            o_intra = _dot(a.astype(BF16), vb, NN)
            yield
            st = state[hh]
            st_ref[hh, c] = st
            o_ref[rows, sl] = o_intra + _dot(qe, st.astype(BF16), NT)
            state[hh] = st * jnp.exp(tot) + kv
            yield

        _run_staged([unit(hh, j) for hh in range(hpb) for j in range(SCAN_CB)])
        for hh in range(hpb):
            s_scr[hh] = state[hh]

    ntb = t // SCAN_TB
    col = pl.BlockSpec((SCAN_TB, hpb * DH), lambda h, i: (_block_order(i, ntb, rev), h))
    return pl.pallas_call(
        body, name="scan_fwd_" + ("bw" if rev else "fw"), grid=(NH // hpb, ntb),
        in_specs=[col] * 4,
        out_specs=[col, pl.BlockSpec((hpb, SCAN_CB, DH, DH), lambda h, i: (h, _block_order(i, ntb, rev), 0, 0))],
        out_shape=[jax.ShapeDtypeStruct((t, D), F32), jax.ShapeDtypeStruct((NH, nc, DH, DH), F32)],
        scratch_shapes=[pltpu.VMEM((hpb, DH, DH), F32), pltpu.VMEM((hpb * SCAN_CB, CHUNK, DH), F32)],
        compiler_params=_cp("parallel", "arbitrary"),
    )(q, k, v, g)


def _scan_bwd(q, k, v, g, do, states, rev):
    t = q.shape[0]
    nc = t // CHUNK
    hpb = SCAN_HEADS_BWD

    def body(q_ref, k_ref, v_ref, g_ref, do_ref, st_ref, dq_ref, dk_ref, dv_ref, dg_ref, ds_scr, b_scr):
        consts = _scan_consts(rev)
        _, tri_t, lv, diag, diag_t = consts
        masks = [(lvl[0], lvl[1]) for lvl in lv] + [(diag, diag_t)]
        @pl.when(pl.program_id(1) == 0)
        def _():
            ds_scr[...] = jnp.zeros_like(ds_scr)

        tri = consts[0]
        dstate = {hh: ds_scr[hh] for hh in range(hpb)}

        def unit(hh, jj):
            sl = slice(hh * DH, (hh + 1) * DH)
            c = _chunk_in_block(SCAN_CB - 1 - jj, rev)
            rows = slice(c * CHUNK, (c + 1) * CHUNK)
            b_ref = b_scr.at[hh * SCAN_CB + jj]
            qc, kc, vc, gc = q_ref[rows, sl], k_ref[rows, sl], v_ref[rows, sl], g_ref[rows, sl]
            dob = do_ref[rows, sl].astype(BF16)
            vb = vc.astype(BF16)
            cum = _split_dot(tri, gc)
            b_ref[...] = cum
            da = _dot(dob, vb, NT)
            da_t = _dot(vb, dob, NT)
            yield
            terms = _chunk_terms(cum, b_ref, consts, rev)
            qf, kf = qc.astype(F32), kc.astype(F32)
            xs = [(jnp.where(tside, qf, kf) * w).astype(BF16) for w, (_, _, tside, _) in zip(terms[:-1], lv)]
            wqd, wkd = terms[-1]
            qdb, kdb = (qf * wqd).astype(BF16), (kf * wkd).astype(BF16)
            tot = _colsum(gc)
            e_tot = jnp.exp(tot)
            e_b = jnp.exp(cum)
            e_t = jnp.exp(tot - cum)
            qeb = (qf * e_b).astype(BF16)
            keb = (kf * e_t).astype(BF16)
            dsym = [(jnp.where(m, da, 0.0) + jnp.where(m_t, da_t, 0.0)).astype(BF16) for m, m_t in masks[:-1]]
            dad = (jnp.where(diag, da, 0.0).astype(BF16), jnp.where(diag_t, da_t, 0.0).astype(BF16))
            yield
            sym = [_dot(x, x, NT) for x in xs]
            dxs = [_dot(d, x, NN) for d, x in zip(dsym, xs)]
            at_d = _dot(kdb, qdb, NT)
            dqt_d = _dot(dad[0], kdb, NN)
            dkt_d = _dot(dad[1], qdb, NN)
            qd = _dot(dob, qeb, TN)
            yield
            a_t = jnp.where(diag_t, at_d, 0.0)
            dq = dqt_d * wqd
            dk = dkt_d * wkd
            db = dqt_d * qdb.astype(F32) - dkt_d * kdb.astype(F32)
            for s, dx, x, w, (_, m_t, tside, sgn) in zip(sym, dxs, xs, terms[:-1], lv):
                a_t = a_t + jnp.where(m_t, s, 0.0)
                dxw = dx * w
                dq = dq + jnp.where(tside, dxw, 0.0)
                dk = dk + jnp.where(tside, 0.0, dxw)
                db = db + (dx * x.astype(F32)) * sgn
            dv_intra = _dot(a_t.astype(BF16), dob, NN)
            st = st_ref[hh, c]
            stb = st.astype(BF16)
            dqe = _dot(dob, stb, NN)
            yield
            dst = dstate[hh]
            dstb = dst.astype(BF16)
            dstate[hh] = dst * e_tot + qd
            dv_ref[rows, sl] = (dv_intra + _dot(keb, dstb, NT)).astype(BF16)
            dke = _dot(vb, dstb, NN)
            yield
            qe = qeb.astype(F32)
            ke = keb.astype(F32)
            dq_ref[rows, sl] = (dq + dqe * e_b).astype(BF16)
            dk_ref[rows, sl] = (dk + dke * e_t).astype(BF16)
            db = db + dqe * qe - dke * ke
            dtot = _colsum(dstb.astype(F32) * stb.astype(F32)) * e_tot + _colsum(dke * ke)
            dg_ref[rows, sl] = _split_dot(tri_t, db) + dtot
            yield

        _run_staged([unit(hh, jj) for hh in range(hpb) for jj in range(SCAN_CB)])
        for hh in range(hpb):
            ds_scr[hh] = dstate[hh]

    ntb = t // SCAN_TB
    blk = lambda i: _block_order(ntb - 1 - i, ntb, rev)
    col = pl.BlockSpec((SCAN_TB, hpb * DH), lambda h, i: (blk(i), h))
    out = jax.ShapeDtypeStruct((t, D), F32)
    outb = jax.ShapeDtypeStruct((t, D), BF16)
    return pl.pallas_call(
        body, name="scan_bwd_" + ("bw" if rev else "fw"), grid=(NH // hpb, ntb),
        in_specs=[col] * 5 + [pl.BlockSpec((hpb, SCAN_CB, DH, DH), lambda h, i: (h, blk(i), 0, 0))],
        out_specs=[col] * 4,
        out_shape=[outb] * 3 + [out],
        scratch_shapes=[pltpu.VMEM((hpb, DH, DH), F32), pltpu.VMEM((hpb * SCAN_CB, CHUNK, DH), F32)],
        compiler_params=_cp("parallel", "arbitrary"),
    )(q, k, v, g, do, states)


W_IN_GRAD_CHUNKS = (("a", (0, 512)), ("b", (0, 256)), ("b", (256, 512)))
W_IN_REF = 6688
W_IN_PAD = 896
W_IN_PIECE = 256


def _assemble_w_in(g, rows, prev, name):
    n, r, wp = g.shape
    tr = W_IN_PIECE
    tiles = wp // DH
    first = rows[0] // tr

    def body(g_ref, *refs):
        o_ref = refs[-1]
        lane = lax.broadcasted_iota(jnp.int32, (tr, DH), 1)
        for t in range(W_IN_COLS // DH):
            acc = None
            for j in range(n):
                c = DH * t - W_IN_SHARD * j
                if c <= -DH or c >= W_IN_SHARD:
                    continue
                k, s = divmod(c, DH)
                lo = g_ref[j, :, k * DH:(k + 1) * DH] if 0 <= k < tiles else None
                hi = g_ref[j, :, (k + 1) * DH:(k + 2) * DH] if s and 0 <= k + 1 < tiles else None
                if s:
                    zero = jnp.zeros((tr, DH), g.dtype)
                    lo = zero if lo is None else pltpu.roll(lo, DH - s, 1)
                    hi = zero if hi is None else pltpu.roll(hi, DH - s, 1)
                    part = jnp.where(lane < DH - s, lo, hi)
                else:
                    part = lo
                acc = part if acc is None else acc + part
            o_ref[:, t * DH:(t + 1) * DH] = jnp.zeros((tr, DH), g.dtype) if acc is None else acc

    held = [] if prev is None else [prev]
    return pl.pallas_call(
        body, name=name, grid=((rows[1] - rows[0]) // tr,),
        in_specs=[pl.BlockSpec((n, tr, wp), lambda i: (0, first + i, 0))] + [pl.BlockSpec(memory_space=pl.ANY)] * len(held),
        out_specs=pl.BlockSpec((tr, W_IN_COLS), lambda i: (first + i, 0)),
        out_shape=jax.ShapeDtypeStruct((r, W_IN_COLS), g.dtype),
        input_output_aliases={1: 0} if held else {},
        compiler_params=_cp("parallel"),
    )(g, *held)


def _gate_cols(w):
    return jnp.pad(w, ((0, 0), (GOFF, GW - GOFF - D)))


def _gate_rows(w):
    return jnp.pad(w, ((GOFF, GW - GOFF - D), (0, 0)))


def _layout_wgk(w):
    r = w.shape[1]
    top = jnp.concatenate([w[0], jnp.zeros_like(w[0])], axis=1)
    bot = jnp.concatenate([jnp.zeros_like(w[1]), w[1]], axis=1)
    return jnp.concatenate([top, bot, jnp.zeros((DH - 2 * r, D), w.dtype)], axis=0)


def _unlayout_wgk(d, r=16):
    return jnp.stack([d[:r, :HW], d[r:2 * r, HW:]])


def _local_step(z, target, modc, modx, norms, onw, hg_lb, wgk, bgk, get_w_in, get_mix, get_ffn, send):
    n_pre1, n_post1, n_pre2, n_post2 = norms
    t = z[0].shape[0] + z[1].shape[0]
    tm = 1152 if t % 1152 == 0 else 256
    h1 = _prenorm(z, n_pre1, modc, modx, 0, 1, "prenorm1")
    w_in = get_w_in(h1)
    p = _matmul(h1, w_in, NN, BF16, "mm_in", t, 1024, D)
    q, v, k_f, k_b, g_f, g_b = _gates_fwd(p, hg_lb, wgk, bgk)
    o_f, st_f = _scan_fwd(q, k_f, v, g_f, False)
    o_b, st_b = _scan_fwd(q, k_b, v, g_b, True)
    y = _post_fwd(o_f, o_b, p, onw)
    w_br_hg, w_br_gla, w_out = get_mix(y)
    u1, u2, merged = _branch_merge(y, w_br_hg, w_br_gla, p)
    y1 = _matmul(merged, w_out, NN, BF16, "mm_out", tm, 512, GW)
    z1, h2 = _mid_fwd(z, y1, n_post1, n_pre2, modc, modx)
    w_gu_t, w_down = get_ffn(h2)
    u, v_ff, act = _mm_gu_act(h2, w_gu_t[0], w_gu_t[1], "mm_gu", tm)
    y2 = _matmul(act, w_down, NN, BF16, "mm_down", t, 512, D_FF)
    dz, dy2, loss_vec, sm_final = _final(z1, y2, target, n_post2, modc, modx)
    du, dv_ff = _mm_down_dx_act(dy2, w_down, u, v_ff, "mm_down_dx", tm)
    d_w_down = _matmul(act, dy2, TN, BF16, "mm_down_dw", D_FF // 2, 1024, t)
    dh2 = _matmul((du, dv_ff), w_gu_t, NN, BF16, "mm_gu_dx", tm, 512, D_FF)
    d_w_gate_t = _matmul(du, h2, TN, BF16, "mm_gate_dw", D_FF // 2, 1024, t)
    d_w_up_t = _matmul(dv_ff, h2, TN, BF16, "mm_up_dw", D_FF // 2, 1024, t)
    dh2 = send(("w_down", "w_gate_t", "w_up_t"), (d_w_down, d_w_gate_t, d_w_up_t), dh2)
    dz, dy1, sm_mid = _mid_bwd(dh2, dz, z1, y1, n_post1, n_pre2, modc, modx)
    dmerged = _matmul(dy1, w_out, NT, BF16, "mm_out_dx", tm, GW, D)
    d_w_out = _matmul(merged, dy1, TN, BF16, "mm_out_dw", GW, 512, t)
    du1, du2, dgm, dy_hg, dy_gla = _branch_merge_bwd(dmerged, p, u1, u2, w_br_hg, w_br_gla)
    d_w_br_hg = _matmul(y, du1, TN, BF16, "mm_br_hg_dw", HW, GW, t, a_off=0, m_out=HW)
    d_w_br_gla = _matmul(y, du2, TN, BF16, "mm_br_gla_dw", HW, GW, t, a_off=1, m_out=HW)
    dy_hg = send(("w_out", "w_br_hg", "w_br_gla"), (d_w_out, d_w_br_hg, d_w_br_gla), dy_hg)
    do, dgo, sm_post = _post_bwd(dy_hg, dy_gla, o_f, o_b, p, onw)
    dq_f, dk_f, dv_f, dg_f = _scan_bwd(q, k_f, v, g_f, do, st_f, False)
    dq_b, dk_b, dv_b, dg_b = _scan_bwd(q, k_b, v, g_b, do, st_b, True)
    dp, d_lb, d_wgk, d_bgk = _gates_bwd(p, hg_lb, wgk, bgk, dgm, dgo, dq_f, dq_b, dv_f, dv_b, dk_f, dk_b, dg_f, dg_b)
    d_w_in_a = _matmul(h1, dp, TN, BF16, "mm_in_dw_a", 512, 1024, t, a_off=0, m_out=D // 2)
    dp = send(("w_in_a",), (d_w_in_a,), dp)
    d_w_in_b = _matmul(h1, dp, TN, BF16, "mm_in_dw_b", 512, 1024, t, a_off=1, m_out=D // 2)
    dp = send(("w_in_b",), (d_w_in_b,), dp)
    dh1 = _matmul(dp, w_in, NT, BF16, "mm_in_dx", tm, 512, W_IN_COLS // 2)
    grad_x, sm_pre = _pre_bwd(dh1, dz, z, n_pre1, modc, modx)
    return dict(loss_vec=loss_vec, grad_x=grad_x, sm_final=sm_final, sm_mid=sm_mid, sm_post=sm_post, sm_pre=sm_pre,
                d_lb=d_lb, d_wgk=d_wgk, d_bgk=d_bgk)


MESH = pl.DeviceIdType.MESH
ANY = pl.BlockSpec(memory_space=pl.ANY)
N_REL = N_DEV - 1


def _place():
    return lax.axis_index("x"), lax.axis_index("y"), lax.axis_index("c")


def _slot(p):
    return 4 * p[0] + 2 * p[1] + p[2]


HBM = pl.BlockSpec(memory_space=pltpu.HBM)
SEM = pl.BlockSpec(memory_space=pltpu.SEMAPHORE)
EFFECT = pltpu.SideEffectType.DATAFLOW_SIDE_EFFECTING


def _peer_of(x, y, c, k):
    flip = lambda v, bit: 1 - v if bit else v
    return flip(x, k & 4), flip(y, k & 2), flip(c, k & 1)


def _view_whole(src, slot):
    return src


def _view_near(src, slot):
    return src


_view_near.peers = (1, 2, 4, 6)


def _view_near_rows(rows):
    def view(src, slot):
        return src.at[pl.ds(rows[0], rows[1] - rows[0])]
    view.peers = _view_near.peers
    view.land = lambda land, slot: land.at[slot, pl.ds(rows[0], rows[1] - rows[0])]
    return view


def _view_block(src, slot):
    return src.at[slot]


W_IN_SHARD = W_IN_REF // N_DEV


def _view_window(rows):
    def view(src, slot):
        col0 = pl.multiple_of((W_IN_SHARD * slot // DH) * DH, DH)
        return src.at[pl.ds(rows[0], rows[1] - rows[0]), pl.ds(col0, D)]
    return view


def _split_copies(view, srcs, lands, send_sems, recv_sems, local_sems):
    x, y, c = _place()
    me = _slot((x, y, c))
    into = getattr(view, "land", lambda land, slot: land.at[slot])
    local, sends, waits = [], [], []
    for a, (src, land) in enumerate(zip(srcs, lands)):
        local.append(pltpu.make_async_copy(view(src, me), into(land, me), local_sems.at[a]))
        for k in getattr(view, "peers", range(1, N_DEV)):
            peer = _peer_of(x, y, c, k)
            mine = view(src, _slot(peer))
            sems = dict(send_sem=send_sems.at[N_REL * a + k - 1], recv_sem=recv_sems.at[N_REL * a + k - 1],
                        device_id=peer, device_id_type=MESH)
            sends.append(pltpu.make_async_remote_copy(src_ref=mine, dst_ref=into(land, me), **sems))
            waits.append(pltpu.make_async_remote_copy(src_ref=mine, dst_ref=into(land, _slot(peer)), **sems))
    return local, sends, waits


def _split_start(groups, name, after):
    built = []
    for view, srcs, lands in groups:
        lands = [lax.empty(l, s.dtype) if isinstance(l, tuple) else l for l, s in zip(lands, srcs)]
        built.append((view, list(srcs), lands))
    bufs = [b for _, srcs, lands in built for b in srcs + lands]
    nb, ng = len(bufs), len(built)

    def body(*refs):
        buf_refs, sem_refs, token = refs[:nb], refs[nb + 1:nb + 1 + 3 * ng], refs[-1]
        pos = 0
        for i, (view, srcs, _) in enumerate(built):
            n = len(srcs)
            local, sends, _ = _split_copies(view, buf_refs[pos:pos + n], buf_refs[pos + n:pos + 2 * n],
                                            *sem_refs[3 * i:3 * i + 3])
            pos += 2 * n
            for cp in local + sends:
                cp.start()
        token[...] = jnp.zeros_like(token)

    sems = []
    for _, srcs, _ in built:
        n = len(srcs)
        sems += [pltpu.SemaphoreType.DMA((N_REL * n,)), pltpu.SemaphoreType.DMA((N_REL * n,)),
                 pltpu.SemaphoreType.DMA((n,))]
    hbm = lambda a: pltpu.with_memory_space_constraint(a, pltpu.HBM)
    out = pl.pallas_call(
        body, name=name,
        out_shape=(*sems, *[pltpu.HBM(b.shape, b.dtype) for b in bufs], jax.ShapeDtypeStruct((8, DH), F32)),
        in_specs=[HBM] * nb + [ANY],
        out_specs=(*([SEM] * (3 * ng)), *([HBM] * nb), pl.BlockSpec(memory_space=pltpu.VMEM)),
        input_output_aliases={i: 3 * ng + i for i in range(nb)},
        compiler_params=pltpu.CompilerParams(has_side_effects=EFFECT),
    )(*[hbm(b) for b in bufs], after)
    handles, pos = [], 3 * ng
    for i, (view, srcs, _) in enumerate(built):
        n = len(srcs)
        handles.append(dict(view=view, n=n, sems=out[3 * i:3 * i + 3], srcs=list(out[pos:pos + n]),
                            lands=list(out[pos + n:pos + 2 * n])))
        pos += 2 * n
    return handles, out[-1]


def _split_wait(handle, name, after, srcs=None, lands=None):
    view, n, sems = handle["view"], handle["n"], handle["sems"]
    srcs = handle["srcs"] if srcs is None else srcs
    lands = handle["lands"] if lands is None else lands
    afters = list(after) if isinstance(after, (list, tuple)) else [after]

    def body(*refs):
        src_refs, land_refs = refs[:n], refs[n:2 * n]
        send_sems, recv_sems, local_sems = refs[2 * n:2 * n + 3]
        local, _, waits = _split_copies(view, src_refs, land_refs, send_sems, recv_sems, local_sems)
        for cp in waits:
            cp.wait_send()
            cp.wait_recv()
        for cp in local:
            cp.wait()

    out = pl.pallas_call(
        body, name=name,
        out_shape=(*[pltpu.HBM(s.shape, s.dtype) for s in srcs], *[pltpu.HBM(l.shape, l.dtype) for l in lands]),
        in_specs=[HBM] * (2 * n) + [SEM, SEM, SEM] + [ANY] * len(afters),
        out_specs=tuple([HBM] * (2 * n)),
        input_output_aliases={i: i for i in range(2 * n)},
        compiler_params=pltpu.CompilerParams(has_side_effects=EFFECT),
    )(*srcs, *lands, *sems, *afters)
    handle["srcs"] = list(out[:n])
    return list(out[n:])


def _tie(x, token, name):
    def body(x_ref, t_ref, o_ref):
        pass

    return pl.pallas_call(
        body, name=name, out_shape=jax.ShapeDtypeStruct(x.shape, x.dtype),
        in_specs=[ANY, ANY], out_specs=ANY, input_output_aliases={0: 0},
    )(x, token)


def _forward_to_sibling(land, name, rows):
    def body(land_ref, out_ref, send_sems, recv_sems):
        x, y, c = _place()
        sibling = (x, y, 1 - c)
        chips = [(1 - x, y), (x, 1 - y), (1 - x, 1 - y)]
        piece = pl.ds(rows[0], rows[1] - rows[0])

        def copy(j, core):
            blk = _slot((*chips[j], core))
            return pltpu.make_async_remote_copy(src_ref=land_ref.at[blk, piece], dst_ref=out_ref.at[blk, piece],
                                                send_sem=send_sems.at[j], recv_sem=recv_sems.at[j],
                                                device_id=sibling, device_id_type=MESH)

        sends = [copy(j, c) for j in range(3)]
        for cp in sends:
            cp.start()
        for j in range(3):
            copy(j, 1 - c).wait_recv()
        for cp in sends:
            cp.wait_send()

    return pl.pallas_call(
        body, name=name, in_specs=[ANY], out_specs=ANY, input_output_aliases={0: 0},
        out_shape=jax.ShapeDtypeStruct(land.shape, land.dtype),
        scratch_shapes=[pltpu.SemaphoreType.DMA((3,)), pltpu.SemaphoreType.DMA((3,))],
    )(land)


def _mod_fwd(a, w, b):
    def body(a_ref, w_ref, b_ref, o_ref):
        o_ref[...] = _dot(_silu(a_ref[...]), w_ref[...], NN, precision=HI) + b_ref[...]

    return pl.pallas_call(
        body, name="mod_fwd", out_shape=jax.ShapeDtypeStruct((a.shape[0], w.shape[1]), F32),
        compiler_params=pltpu.CompilerParams(vmem_limit_bytes=VMEM_LIMIT),
    )(a, w, b)


def _mod_bwd(a, d, w):
    def body(a_ref, d_ref, w_ref, dw_ref, dc_ref):
        av = a_ref[...]
        dv = d_ref[...]
        dw_ref[...] = _dot(_silu(av), dv, TN, precision=HI)
        da = _dot(dv[0:8, :], w_ref[...], NT, precision=HI) * _dsilu(av[0:8, :])
        row = lax.broadcasted_iota(jnp.int32, da.shape, 0)
        dc_ref[...] = jnp.where(row == 0, da, 0.0)

    return pl.pallas_call(
        body, name="mod_bwd",
        out_shape=[jax.ShapeDtypeStruct(w.shape, F32), jax.ShapeDtypeStruct((8, w.shape[0]), F32)],
        compiler_params=pltpu.CompilerParams(vmem_limit_bytes=VMEM_LIMIT),
    )(a, d, w)


def _sum_devices(g):
    def body(g_ref, o_ref):
        acc = g_ref[0]
        for i in range(1, g.shape[0]):
            acc = acc + g_ref[i]
        o_ref[...] = acc

    return pl.pallas_call(body, name="sum_devices_%d" % g.shape[1],
                          out_shape=jax.ShapeDtypeStruct(g.shape[1:], F32))(g)


def _sum_windows(g, name):
    n, r, c = g.shape
    tr = 128

    def body(g_ref, o_ref):
        x, y, cc = _place()
        lane0 = (W_IN_SHARD * _slot((x, y, cc))) % DH
        acc = g_ref[0].astype(F32)
        for i in range(1, n):
            acc = acc + g_ref[i].astype(F32)
        o_ref[...] = pltpu.roll(acc, (c - lane0) % c, 1).T

    return pl.pallas_call(
        body, name=name, grid=(r // tr,),
        in_specs=[pl.BlockSpec((n, tr, c), lambda i: (0, i, 0))],
        out_specs=pl.BlockSpec((c, tr), lambda i: (0, i)),
        out_shape=jax.ShapeDtypeStruct((c, r), F32),
        compiler_params=_cp("parallel"),
    )(g)


def _adam_rows(r, c, n):
    budget = 6 * 1024 * 1024
    best = None
    for tr in range(16, r + 1, 16):
        if r % tr == 0 and tr * c * (2 * n + 28) <= budget:
            best = tr
    return best if best is not None else r


def _adamw(g, w, m, v, name):
    n, r, c = g.shape
    tr = _adam_rows(r, c, n)
    bc1 = 1.0 - ADAM_B1 ** ADAM_STEP
    bc2 = 1.0 - ADAM_B2 ** ADAM_STEP

    def body(g_ref, w_ref, m_ref, v_ref, go_ref, d_ref, mo_ref, vo_ref):
        grad = g_ref[0].astype(F32)
        for i in range(1, n):
            grad = grad + g_ref[i].astype(F32)
        go_ref[...] = grad
        m_new = ADAM_B1 * m_ref[...] + (1.0 - ADAM_B1) * grad
        v_new = ADAM_B2 * v_ref[...] + (1.0 - ADAM_B2) * (grad * grad)
        mo_ref[...] = m_new
        vo_ref[...] = v_new
        d_ref[...] = -ADAM_LR * ((m_new / bc1) / (jnp.sqrt(v_new / bc2) + ADAM_EPS) + ADAM_WD * w_ref[...])

    blk = pl.BlockSpec((tr, c), lambda i: (i, 0))
    out = jax.ShapeDtypeStruct((r, c), F32)
    return pl.pallas_call(
        body, name=name, grid=(r // tr,),
        in_specs=[pl.BlockSpec((n, tr, c), lambda i: (0, i, 0)), blk, blk, blk],
        out_specs=[blk] * 4, out_shape=[out] * 4,
        compiler_params=_cp("parallel"),
    )(g, w, m, v)


ADAM_ROWS3 = 168


def _adam_math(grad, w, m, v):
    bc1 = 1.0 - ADAM_B1 ** ADAM_STEP
    bc2 = 1.0 - ADAM_B2 ** ADAM_STEP
    m_new = ADAM_B1 * m + (1.0 - ADAM_B1) * grad
    v_new = ADAM_B2 * v + (1.0 - ADAM_B2) * (grad * grad)
    delta = -ADAM_LR * ((m_new / bc1) / (jnp.sqrt(v_new / bc2) + ADAM_EPS) + ADAM_WD * w)
    return delta, m_new, v_new


def _adamw_rows3(g, w3, m3, v3, name):
    r, _, c = w3.shape
    n = ADAM_ROWS3
    starts = list(range(0, r - n, n)) + [r - n]

    def body(g_hbm, w_hbm, m_hbm, v_hbm, go_hbm, d_hbm, mo_hbm, vo_hbm, gbuf, ibuf, obuf, in_sems, out_sems):
        def fetch(p):
            r0, slot = starts[p], p % 2
            g0 = (r0 // 8) * 8
            cps = [pltpu.make_async_copy(g_hbm.at[pl.ds(g0, n + 8)], gbuf.at[slot], in_sems.at[slot, 0])]
            cps += [pltpu.make_async_copy(h.at[pl.ds(r0, n), 0], ibuf.at[slot, k], in_sems.at[slot, 1 + k])
                    for k, h in enumerate((w_hbm, m_hbm, v_hbm))]
            for cp in cps:
                cp.start()
            return cps

        pending, outs = fetch(0), []
        for p, r0 in enumerate(starts):
            slot = p % 2
            nxt = fetch(p + 1) if p + 1 < len(starts) else []
            for cp in pending:
                cp.wait()
            grad = gbuf[slot, pl.ds(r0 - (r0 // 8) * 8, n), :]
            delta, m_new, v_new = _adam_math(grad, ibuf[slot, 0], ibuf[slot, 1], ibuf[slot, 2])
            for cp in outs:
                cp.wait()
            for k, val in enumerate((grad, delta, m_new, v_new)):
                obuf[slot, k] = val
            outs = [pltpu.make_async_copy(obuf.at[slot, k], h.at[pl.ds(r0, n), 0], out_sems.at[slot, k])
                    for k, h in enumerate((go_hbm, d_hbm, mo_hbm, vo_hbm))]
            for cp in outs:
                cp.start()
            pending = nxt
        for cp in outs:
            cp.wait()

    out = jax.ShapeDtypeStruct(w3.shape, F32)
    return pl.pallas_call(
        body, name=name, in_specs=[ANY] * 4, out_specs=[ANY] * 4, out_shape=[out] * 4,
        scratch_shapes=[pltpu.VMEM((2, n + 8, c), F32), pltpu.VMEM((2, 3, n, c), F32), pltpu.VMEM((2, 4, n, c), F32),
                        pltpu.SemaphoreType.DMA((2, 4)), pltpu.SemaphoreType.DMA((2, 4))],
        compiler_params=pltpu.CompilerParams(vmem_limit_bytes=VMEM_LIMIT),
    )(g, w3, m3, v3)


def kernel(x, c, ctx, c_ctx, w_mod, b_mod, norm_pre1, norm_post1, norm_pre2, norm_post2, w_in, hg_lb, hg_onorm, gla_w_gk, gla_b_gk, gla_onorm, w_br_hg, w_br_gla, w_out, w_ff_gate, w_ff_up, w_ff_down, loss_target, m_c_ctx, m_w_mod, m_b_mod, m_norm_pre1, m_norm_post1, m_norm_pre2, m_norm_post2, m_w_in, m_hg_lb, m_hg_onorm, m_gla_w_gk, m_gla_b_gk, m_gla_onorm, m_w_br_hg, m_w_br_gla, m_w_out, m_w_ff_gate, m_w_ff_up, m_w_ff_down, v_c_ctx, v_w_mod, v_b_mod, v_norm_pre1, v_norm_post1, v_norm_pre2, v_norm_post2, v_w_in, v_hg_lb, v_hg_onorm, v_gla_w_gk, v_gla_b_gk, v_gla_onorm, v_w_br_hg, v_w_br_gla, v_w_out, v_w_ff_gate, v_w_ff_up, v_w_ff_down):
    xi, yi, ci = lax.axis_index("x"), lax.axis_index("y"), lax.axis_index("c")
    me = 4 * xi + 2 * yi + ci
    t = CTX + x.shape[1]

    w_in_pieces, w_in_state = [], {}

    def w_in_piece(i):
        return (_view_near_rows((i * W_IN_PIECE, (i + 1) * W_IN_PIECE)), w_in_state["src"], w_in_state["land"])

    def started_w_in(handle):
        w_in_state.update(src=handle["srcs"], land=handle["lands"])
        w_in_pieces.append(handle)

    tr_ = lambda a: jnp.swapaxes(a[0], 0, 1)
    w_in_bf = jnp.pad(w_in[0].astype(BF16), ((0, 0), (0, W_IN_PAD - W_IN_SHARD)))
    w_in_state.update(src=[w_in_bf], land=[lax.empty((N_DEV,) + w_in_bf.shape, BF16)])
    gathered = lambda arrs: [(N_DEV,) + a.shape for a in arrs]
    whole = lambda arrs: (_view_whole, arrs, gathered(arrs))
    small_in = [c, hg_lb, gla_w_gk[0], gla_b_gk[0]]
    (small_handle, piece), tok = _split_start([whole(small_in), w_in_piece(0)], "ag_small_start", c)
    started_w_in(piece)
    c_all, lb_g, wgk_g, bgk_g = _split_wait(small_handle, "ag_small_wait", tok)
    big = [w_in[0], w_br_hg[0], w_br_gla[0], w_out[0], tr_(w_ff_gate), tr_(w_ff_up), w_ff_down[0]]
    big_bf = [None] + [w.astype(BF16) for w in big[1:]]
    cols = lambda g: jnp.transpose(g, (1, 0, 2)).reshape(g.shape[1], N_DEV * g.shape[2])

    def get_w_in(after):
        w_full = None
        for i, handle in enumerate(w_in_pieces):
            rows = (i * W_IN_PIECE, (i + 1) * W_IN_PIECE)
            land = _split_wait(handle, "ag_w_in_wait%d" % i, after if w_full is None else [after, w_full],
                               srcs=w_in_state["src"], lands=w_in_state["land"])
            land = [_forward_to_sibling(land[0], "ag_w_in_forward%d" % i, rows)]
            w_in_state.update(src=handle["srcs"], land=land)
            w_full = _assemble_w_in(land[0], rows, w_full, "assemble_w_in%d" % i)
        return w_full

    def get_mix(after):
        g_brh, g_brg, g_out = _split_wait(mix_handle, "ag_mix_wait", after)
        return _gate_cols(cols(g_brh)), _gate_cols(cols(g_brg)), _gate_rows(g_out.reshape(D, D))

    def get_ffn(after):
        g_gate, g_up, g_down = _split_wait(ffn_handle, "ag_ffn_wait", after)
        return (g_gate.reshape(D_FF, D), g_up.reshape(D_FF, D)), g_down.reshape(D_FF, D)

    hg_lb_full = jnp.transpose(lb_g, (1, 2, 0, 3)).reshape(2, 2, HW)
    wgk_k = _layout_wgk(jnp.transpose(wgk_g, (1, 2, 0, 3)).reshape(2, 16, HW)).astype(BF16)
    bgk_k = jnp.transpose(bgk_g, (1, 0, 2)).reshape(1, D)
    onw = jnp.concatenate([jnp.tile(hg_onorm, (1, NH // 2)), jnp.tile(gla_onorm, (1, NH // 2))], axis=1)

    n_mod = w_mod.shape[2]
    a9 = jnp.concatenate([c_ctx[None], c_all[:, 0], jnp.zeros((16 - 1 - N_DEV, D), F32)], axis=0)
    b_loc = lax.dynamic_slice(b_mod, (0, me * n_mod), (1, n_mod))
    s_loc = _mod_fwd(a9, w_mod[0], b_loc)
    (mod_handle, piece), tok = _split_start([whole([s_loc]), w_in_piece(1)], "ag_mod_start", s_loc)
    started_w_in(piece)
    for i in range(2, D // W_IN_PIECE):
        (piece,), tok = _split_start([w_in_piece(i)], "ag_w_in_start%d" % i, tok)
        started_w_in(piece)
    s_all, = _split_wait(mod_handle, "ag_mod_wait", tok)
    mod_all = jnp.transpose(s_all, (1, 0, 2)).reshape(16, N_DEV * n_mod)
    pad8 = lambda m: jnp.concatenate([m.reshape(6, D), jnp.zeros((2, D), F32)], axis=0)
    modc = pad8(mod_all[0])
    modx = pad8(lax.dynamic_slice(mod_all, (1 + me, 0), (1, N_DEV * n_mod))[0])

    (mix_handle, ffn_handle), tok = _split_start([whole(big_bf[1:4]), whole(big_bf[4:])], "ag_big_start", s_all)

    z = (ctx[0], x[0])
    modx = _tie(modx, tok, "tie_mod")
    norms = (norm_pre1, norm_post1, norm_pre2, norm_post2)
    shard = lambda d: jnp.transpose(d.reshape(d.shape[0], N_DEV, -1), (1, 0, 2)).astype(BF16)
    rowshard = lambda d: d.reshape(N_DEV, d.shape[0] // N_DEV, d.shape[1]).astype(BF16)
    sent, w_in_grad = [], {}

    def w_in_chunk(i):
        half, rows = W_IN_GRAD_CHUNKS[i]
        return (_view_window(rows), w_in_grad[half], [(N_DEV, rows[1] - rows[0], D)])

    def sent_w_in(i, handle):
        w_in_grad[W_IN_GRAD_CHUNKS[i][0]] = handle["srcs"]
        sent.append(("w_in%d" % i, ["w_in#%d" % i], handle))

    def send(names, grads, x_after):
        if names == ("w_in_a",):
            w_in_grad["a"] = list(grads)
            (handle,), tok = _split_start([w_in_chunk(0)], "grads_w_in0_start", x_after)
            sent_w_in(0, handle)
            return _tie(x_after, tok, "tie_w_in0")
        if names == ("w_in_b",):
            w_in_grad["b"] = list(grads)
            return x_after
        arrs, leaves = [], []
        for nm, g in zip(names, grads):
            if nm in ("w_gate_t", "w_up_t"):
                arrs.append(rowshard(g))
                leaves.append({"w_gate_t": "w_ff_gate", "w_up_t": "w_ff_up"}[nm])
            elif nm == "w_down":
                arrs.append(rowshard(g))
                leaves.append("w_ff_down")
            elif nm == "w_out":
                arrs.append(rowshard(g[GOFF:GOFF + D]))
                leaves.append(nm)
            else:
                arrs.append(shard(g[:, GOFF:GOFF + D]))
                leaves.append(nm)
        (handle,), tok = _split_start([(_view_block, arrs, [a.shape for a in arrs])], "grads_%s_start" % names[0],
                                      x_after)
        sent.append((names[0], leaves, handle))
        return _tie(x_after, tok, "tie_" + names[0])

    r = _local_step(z, loss_target[0], modc, modx, norms, onw, hg_lb_full, wgk_k, bgk_k,
                    get_w_in, get_mix, get_ffn, send)
    grad_x = r["grad_x"][None]

    sm_pre, sm_mid, sm_fin = r["sm_pre"], r["sm_mid"], r["sm_final"]
    dmodc = jnp.stack([sm_pre[0], sm_pre[2], sm_mid[4], sm_mid[0], sm_mid[2], sm_fin[0]]).reshape(-1)
    dmodx = jnp.stack([sm_pre[1], sm_pre[3], sm_mid[5], sm_mid[1], sm_mid[3], sm_fin[1]]).reshape(-1)
    on = r["sm_post"][0].reshape(NH, DH)
    pieces = [dmodc, dmodx, sm_pre[4], sm_mid[7], sm_mid[6], sm_fin[2], on[:NH // 2].sum(0), on[NH // 2:].sum(0),
              r["d_lb"][:2].reshape(-1), _unlayout_wgk(r["d_wgk"]).reshape(-1), r["d_bgk"][0]]
    loss_local = (0.5 / D) * jnp.sum(r["loss_vec"])
    pieces.append(jnp.concatenate([loss_local.reshape(1), jnp.zeros((DH - 1,), F32)]))
    sizes = [p.shape[0] for p in pieces]
    pack = jnp.concatenate(pieces).reshape(-1, DH)
    moms = [(m_w_in, v_w_in), (m_w_br_hg, v_w_br_hg), (m_w_br_gla, v_w_br_gla), (m_w_out, v_w_out),
            (m_w_ff_gate, v_w_ff_gate), (m_w_ff_up, v_w_ff_up), (m_w_ff_down, v_w_ff_down)]
    names = ["w_in", "w_br_hg", "w_br_gla", "w_out", "w_ff_gate", "w_ff_up", "w_ff_down"]
    wmv = {nm: (w, m, v) for nm, w, (m, v) in zip(names, big, moms)}
    res = {}

    def update(nm):
        w, m, v = wmv[nm]
        if nm in ("w_ff_gate", "w_ff_up"):
            outs = _adamw(recv[nm], w, tr_(m), tr_(v), "adamw_" + nm)
            res[nm] = [jnp.swapaxes(o, 0, 1)[None] for o in outs]
        else:
            res[nm] = [o[None] for o in _adamw(recv[nm], w, m[0], v[0], "adamw_" + nm)]

    (small_handle, handle), tok = _split_start([whole([pack]), w_in_chunk(1)], "small_grads_start", pack)
    sent_w_in(1, handle)
    recv = {}
    for first, leaves, handle in sent:
        if not first.startswith("w_in"):
            recv.update(zip(leaves, _split_wait(handle, "grads_%s_wait" % first, tok)))
    update("w_ff_gate")
    update("w_ff_up")
    pack_all, = _split_wait(small_handle, "small_grads_wait", [res["w_ff_gate"][0], res["w_ff_up"][0]])
    tot = _sum_devices(pack_all).reshape(-1)
    offs = [sum(sizes[:i]) for i in range(len(sizes))]
    part = lambda i: tot[offs[i]:offs[i] + sizes[i]]
    dmodc_t, dmodx_t = part(0), part(1)
    g_b_mod = (dmodc_t + dmodx_t)[None]
    g_norms = [part(i)[None] for i in (2, 3, 4, 5)]
    g_hg_on, g_gla_on = part(6)[None], part(7)[None]
    lb0 = lax.dynamic_slice(part(8).reshape(2, HW), (0, me * (HW // N_DEV)), (2, HW // N_DEV))
    g_hg_lb = jnp.stack([lb0, -lb0])
    g_wgk = lax.dynamic_slice(part(9).reshape(2, 16, HW), (0, 0, me * (HW // N_DEV)), (2, 16, HW // N_DEV))[None]
    g_bgk = lax.dynamic_slice(part(10).reshape(2, HW), (0, me * (HW // N_DEV)), (2, HW // N_DEV))[None]
    loss = part(11)[0]

    dmx_all = pack_all.reshape(N_DEV, -1)[:, sizes[0]:sizes[0] + sizes[1]]
    d9 = jnp.concatenate([lax.dynamic_slice(dmodc_t[None], (0, me * n_mod), (1, n_mod)),
                          lax.dynamic_slice(dmx_all, (0, me * n_mod), (N_DEV, n_mod)),
                          jnp.zeros((16 - 1 - N_DEV, n_mod), F32)], axis=0)
    g_w_mod, dcc_part = _mod_bwd(a9, d9, w_mod[0])
    (cctx_handle, handle), tok = _split_start([whole([dcc_part]), w_in_chunk(2)], "c_ctx_start", dcc_part)
    sent_w_in(2, handle)
    recv["w_ff_down"] = _tie(recv["w_ff_down"], tok, "tie_down")
    update("w_ff_down")
    res["w_mod"] = [o[None] for o in _adamw(g_w_mod[None], w_mod[0], m_w_mod[0], v_w_mod[0], "adamw_w_mod")]
    for nm in ("w_out", "w_br_hg", "w_br_gla"):
        update(nm)
    dcc_all, = _split_wait(cctx_handle, "c_ctx_wait", [res["w_ff_down"][0], res["w_mod"][0]])
    g_c_ctx = _sum_devices(dcc_all)[0]

    small = [("c_ctx", c_ctx, m_c_ctx, v_c_ctx, g_c_ctx), ("b_mod", b_mod, m_b_mod, v_b_mod, g_b_mod),
             ("norm_pre1", norm_pre1, m_norm_pre1, v_norm_pre1, g_norms[0]),
             ("norm_post1", norm_post1, m_norm_post1, v_norm_post1, g_norms[1]),
             ("norm_pre2", norm_pre2, m_norm_pre2, v_norm_pre2, g_norms[2]),
             ("norm_post2", norm_post2, m_norm_post2, v_norm_post2, g_norms[3]),
             ("hg_lb", hg_lb, m_hg_lb, v_hg_lb, g_hg_lb), ("hg_onorm", hg_onorm, m_hg_onorm, v_hg_onorm, g_hg_on),
             ("gla_w_gk", gla_w_gk, m_gla_w_gk, v_gla_w_gk, g_wgk), ("gla_b_gk", gla_b_gk, m_gla_b_gk, v_gla_b_gk, g_bgk),
             ("gla_onorm", gla_onorm, m_gla_onorm, v_gla_onorm, g_gla_on)]
    flat = lambda k: jnp.concatenate([s[k].reshape(-1) for s in small]).reshape(-1, DH)
    outs = _adamw(flat(4)[None], flat(1), flat(2), flat(3), "adamw_small")
    off = 0
    for nm, w, _, _, _ in small:
        res[nm] = [o.reshape(-1)[off:off + w.size].reshape(w.shape) for o in outs]
        off += w.size

    done = [res[nm][0] for nm in names[1:]] + [res["w_mod"][0], outs[0]]
    sums = []
    for i, (first, leaves, handle) in enumerate(s for s in sent if s[0].startswith("w_in")):
        half = W_IN_GRAD_CHUNKS[i][0]
        land, = _split_wait(handle, "grads_%s_wait" % first, done, srcs=w_in_grad[half])
        w_in_grad[half] = handle["srcs"]
        sums.append(_sum_windows(land, "sum_windows%d" % i))
    major = lambda a: jnp.transpose(a, (2, 0, 1))
    outs = _adamw_rows3(jnp.concatenate(sums, axis=1), major(w_in), major(m_w_in), major(v_w_in), "adamw_w_in")
    res["w_in"] = [jnp.transpose(o, (1, 2, 0)) for o in outs]

    order = ["c_ctx", "w_mod", "b_mod", "norm_pre1", "norm_post1", "norm_pre2", "norm_post2", "w_in", "hg_lb",
             "hg_onorm", "gla_w_gk", "gla_b_gk", "gla_onorm", "w_br_hg", "w_br_gla", "w_out", "w_ff_gate", "w_ff_up",
             "w_ff_down"]
    return (loss, grad_x, *[res[n][k] for k in range(4) for n in order])
```

```python
import functools

import jax
import jax.numpy as jnp
from jax import lax
from jax.experimental import pallas as pl
from jax.experimental.pallas import tpu as pltpu

F32 = jnp.float32
BF16 = jnp.bfloat16
HI = lax.Precision.HIGHEST

N_DEV = 8
D = 1024
CTX = 256
HW = 512
DH = 128
NH = 8
D_FF = 2816
EPS = 1e-6
GLA_NORM = 16.0
CHUNK = 64
TR = 256
NCT = CTX // TR
W_IN_COLS = 7168
MAIN0 = 0
LR0 = 4608
GW = 1152
GOFF = 32
GATE_HG0 = LR0
GATE_GLA0 = LR0 + D
LEVELS = (32, 16, 8)
EXP_CLAMP = 80.0
VMEM_LIMIT = 48 * 1024 * 1024

ADAM_LR, ADAM_B1, ADAM_B2, ADAM_EPS, ADAM_WD, ADAM_STEP = 0.001, 0.9, 0.999, 1e-08, 0.01, 10


def _cp(*sem):
    return pltpu.CompilerParams(dimension_semantics=sem, vmem_limit_bytes=VMEM_LIMIT)


def _sig(x):
    return jax.nn.sigmoid(x)


def _silu(x):
    return x * _sig(x)


def _dsilu(x):
    s = _sig(x)
    return s * (1.0 + x * (1.0 - s))


def _rstd(x):
    return lax.rsqrt(jnp.mean(x * x, axis=-1, keepdims=True) + EPS)


def _rms_bwd(a, y, r):
    return r * (a - y * (r * r) * jnp.mean(a * y, axis=-1, keepdims=True))


def _colsum(x):
    return jnp.sum(x, axis=0, keepdims=True)


def _dot(a, b, dims, precision=None):
    return lax.dot_general(a, b, (dims, ((), ())), preferred_element_type=F32, precision=precision)


NN = ((1,), (0,))
NT = ((1,), (1,))
TN = ((0,), (0,))

SCAN_HEADS_FWD = 4
SCAN_HEADS_BWD = 4


def _split_dot(m, x):
    mb = m.astype(BF16)
    x1 = x.astype(BF16)
    r1 = x - x1.astype(F32)
    x2 = r1.astype(BF16)
    x3 = (r1 - x2.astype(F32)).astype(BF16)
    return _dot(mb, x1, NN) + _dot(mb, x2, NN) + _dot(mb, x3, NN)


def _matmul(a, b, dims, out_dtype, name, tm, tn, tk, a_off=0, m_out=None):
    a_pair = isinstance(a, (tuple, list))
    as_ = list(a) if a_pair else [a]
    a = as_[0]
    pair = isinstance(b, (tuple, list))
    bs = list(b) if pair else [b]
    b1 = bs[0]
    rows = b1.shape[0] * len(bs)
    half = None
    if dims == NN:
        m, k, n = a.shape[0], rows, b1.shape[1]
        a_spec = pl.BlockSpec((tm, tk), lambda i, j, kk: (i, kk + a_off))
        half = b1.shape[0] // tk
        if a_pair:
            assert pair and a.shape[1] == b1.shape[0] and a_off == 0
            a_spec = [pl.BlockSpec((tm, tk), lambda i, j, kk: (i, jnp.minimum(kk, half - 1))),
                      pl.BlockSpec((tm, tk), lambda i, j, kk: (i, jnp.maximum(kk - half, 0)))]
        b_maps = [lambda i, j, kk: (kk, j)] if not pair else [
            lambda i, j, kk: (jnp.minimum(kk, half - 1), j), lambda i, j, kk: (jnp.maximum(kk - half, 0), j)]
        b_specs = [pl.BlockSpec((tk, tn), f) for f in b_maps]
        axis = 2
    elif dims == NT:
        m, k, n = a.shape[0], b1.shape[1], rows
        a_spec = pl.BlockSpec((tm, tk), lambda i, j, kk: (i, kk + a_off))
        half = b1.shape[0] // tn
        b_maps = [lambda i, j, kk: (j, kk)] if not pair else [
            lambda i, j, kk: (jnp.minimum(j, half - 1), kk), lambda i, j, kk: (jnp.maximum(j - half, 0), kk)]
        b_specs = [pl.BlockSpec((tn, tk), f) for f in b_maps]
        axis = 1
    else:
        assert not pair
        m, k = (a.shape[1] if m_out is None else m_out), a.shape[0]
        n = b1.shape[1]
        a_spec = pl.BlockSpec((tk, tm), lambda i, j, kk: (kk, i + a_off))
        b_specs = [pl.BlockSpec((tk, tn), lambda i, j, kk: (kk, j))]
    assert m % tm == 0 and n % tn == 0 and k % tk == 0, (name, m, n, k, tm, tn, tk)
    nk = k // tk
    nb = len(bs)
    na = len(as_)
    assert na == 1 or dims == NN

    def body(*refs):
        a_refs, refs = refs[:na], refs[na:]
        o_ref = refs[nb]
        if pair:
            bv = jnp.where(pl.program_id(axis) < half, refs[0][...], refs[1][...])
        else:
            bv = refs[0][...]
        av = a_refs[0][...] if na == 1 else jnp.where(pl.program_id(2) < half, a_refs[0][...], a_refs[1][...])
        part = _dot(av, bv, dims)
        if nk == 1:
            o_ref[...] = part.astype(o_ref.dtype)
            return
        acc_ref = refs[nb + 1]
        kk = pl.program_id(2)

        @pl.when(kk == 0)
        def _():
            acc_ref[...] = part

        @pl.when(kk > 0)
        def _():
            acc_ref[...] += part

        @pl.when(kk == nk - 1)
        def _():
            o_ref[...] = acc_ref[...].astype(o_ref.dtype)

    return pl.pallas_call(
        body,
        name=name,
        grid=(m // tm, n // tn, nk),
        in_specs=(a_spec if a_pair else [a_spec]) + b_specs,
        out_specs=pl.BlockSpec((tm, tn), lambda i, j, kk: (i, j)),
        out_shape=jax.ShapeDtypeStruct((m, n), out_dtype),
        scratch_shapes=[] if nk == 1 else [pltpu.VMEM((tm, tn), F32)],
        compiler_params=_cp("parallel", "parallel", "arbitrary"),
    )(*as_, *bs)


def _mm_gu_act(h, w_gate_t, w_up_t, name, tm):
    t = h.shape[0]
    tn = D_FF // 2

    def body(a_ref, bg_ref, bu_ref, u_ref, v_ref, act_ref):
        a = a_ref[...]
        u = _dot(a, bg_ref[...], NT)
        v = _dot(a, bu_ref[...], NT)
        u_ref[...] = u.astype(BF16)
        v_ref[...] = v.astype(BF16)
        act_ref[...] = (_silu(u) * v).astype(BF16)

    wspec = pl.BlockSpec((tn, D), lambda i, j: (j, 0))
    ospec = pl.BlockSpec((tm, tn), lambda i, j: (i, j))
    out = jax.ShapeDtypeStruct((t, D_FF), BF16)
    return pl.pallas_call(
        body, name=name, grid=(t // tm, D_FF // tn),
        in_specs=[pl.BlockSpec((tm, D), lambda i, j: (i, 0)), wspec, wspec],
        out_specs=[ospec] * 3, out_shape=[out] * 3,
        compiler_params=_cp("parallel", "parallel"),
    )(h, w_gate_t, w_up_t)


def _mm_down_dx_act(dy, w_down, u, v, name, tm):
    t = dy.shape[0]
    tn = D_FF // 2

    def body(a_ref, b_ref, u_ref, v_ref, du_ref, dv_ref):
        dact = _dot(a_ref[...], b_ref[...], NT)
        u = u_ref[...].astype(F32)
        du_ref[...] = (dact * v_ref[...].astype(F32) * _dsilu(u)).astype(BF16)
        dv_ref[...] = (dact * _silu(u)).astype(BF16)

    ospec = pl.BlockSpec((tm, tn), lambda i, j: (i, j))
    out = jax.ShapeDtypeStruct((t, D_FF), BF16)
    return pl.pallas_call(
        body, name=name, grid=(t // tm, D_FF // tn),
        in_specs=[pl.BlockSpec((tm, D), lambda i, j: (i, 0)), pl.BlockSpec((tn, D), lambda i, j: (j, 0)), ospec, ospec],
        out_specs=[ospec] * 2, out_shape=[out] * 2,
        compiler_params=_cp("parallel", "parallel"),
    )(dy, w_down, u, v)


def _row(c):
    return pl.BlockSpec((TR, c), lambda i: (i, 0))


def _rowcol(width, cb):
    return pl.BlockSpec((TR, width), lambda i: (i, cb))


def _full(shape):
    return pl.BlockSpec(shape, lambda i: (0,) * len(shape))


def _mod_row(mc_ref, mx_ref, k, is_ctx):
    return jnp.where(is_ctx, mc_ref[k:k + 1, :], mx_ref[k:k + 1, :])


def _z_specs():
    return [pl.BlockSpec((TR, D), lambda i: (jnp.minimum(i, NCT - 1), 0)),
            pl.BlockSpec((TR, D), lambda i: (jnp.maximum(i - NCT, 0), 0))]


def _z_tile(c_ref, x_ref, is_ctx):
    return jnp.where(is_ctx, c_ref[...], x_ref[...])


def _acc_row(ref, k, val):
    ref[k:k + 1, :] += val


def _acc_mod(ref, k, is_ctx, val):
    zero = jnp.zeros_like(val)
    ref[k:k + 1, :] += jnp.where(is_ctx, val, zero)
    ref[k + 1:k + 2, :] += jnp.where(is_ctx, zero, val)


def _prenorm(z, nw, modc, modx, i_shift, i_scale, name):
    t = z[0].shape[0] + z[1].shape[0]

    def body(zc_ref, zx_ref, nw_ref, mc_ref, mx_ref, h_ref):
        is_ctx = pl.program_id(0) < NCT
        x = _z_tile(zc_ref, zx_ref, is_ctx)
        n = x * _rstd(x) * nw_ref[...]
        h = n * (1.0 + _mod_row(mc_ref, mx_ref, i_scale, is_ctx)) + _mod_row(mc_ref, mx_ref, i_shift, is_ctx)
        h_ref[...] = h.astype(BF16)

    return pl.pallas_call(
        body, name=name, grid=(t // TR,),
        in_specs=_z_specs() + [_full((1, D)), _full((8, D)), _full((8, D))],
        out_specs=_row(D),
        out_shape=jax.ShapeDtypeStruct((t, D), BF16),
        compiler_params=_cp("parallel"),
    )(*z, nw, modc, modx)


def _hg_lb(lb_ref, d):
    a0 = lb_ref[0, d:d + 1, :]
    a1 = lb_ref[1, d:d + 1, :]
    mx = jnp.maximum(a0, a1)
    e0 = jnp.exp(a0 - mx)
    e1 = jnp.exp(a1 - mx)
    return e0 / (e0 + e1)


def _log_sigmoid(x):
    return jnp.minimum(x, 0.0) - jnp.log(1.0 + jnp.exp(-jnp.abs(x)))


def _gates_fwd(p, hg_lb, wgk, bgk):
    t = p.shape[0]
    seg = lambda j: _rowcol(HW, MAIN0 // HW + j)

    def body(hq_ref, hi_ref, hf_ref, hb_ref, gq_ref, gk_ref, gv_ref, lr_ref, lb_ref, wgk_ref, bgk_ref,
             q_ref, v_ref, kf_ref, kb_ref, gf_ref, gb_ref):
        q_ref[:, :HW] = _silu(hq_ref[...].astype(F32)).astype(BF16)
        q_ref[:, HW:] = (gq_ref[...].astype(F32) * (DH ** -0.5)).astype(BF16)
        v_ref[:, :HW] = hi_ref[...]
        v_ref[:, HW:] = gv_ref[...]
        xg = _dot(lr_ref[...].astype(BF16), wgk_ref[...], NN) + bgk_ref[...]
        for d, (raw_ref, k_ref, g_ref) in enumerate(((hf_ref, kf_ref, gf_ref), (hb_ref, kb_ref, gb_ref))):
            lbd = _hg_lb(lb_ref, d)
            f = lbd + (1.0 - lbd) * _sig(raw_ref[...].astype(F32))
            k_ref[:, :HW] = (1.0 - f).astype(BF16)
            k_ref[:, HW:] = gk_ref[...]
            g_ref[:, :HW] = jnp.log(f)
            g_ref[:, HW:] = _log_sigmoid(xg[:, d * HW:(d + 1) * HW]) * (1.0 / GLA_NORM)

    out = jax.ShapeDtypeStruct((t, D), F32)
    outb = jax.ShapeDtypeStruct((t, D), BF16)
    return pl.pallas_call(
        body, name="gates_fwd", grid=(t // TR,),
        in_specs=[seg(0), seg(1), seg(2), seg(3), seg(5), seg(6), seg(7), _rowcol(DH, LR0 // DH),
                  _full((2, 2, HW)), _full((DH, D)), _full((1, D))],
        out_specs=[_row(D)] * 6,
        out_shape=[outb] * 4 + [out] * 2,
        compiler_params=_cp("parallel"),
    )(p, p, p, p, p, p, p, p, hg_lb, wgk, bgk)


def _post_fwd(o_fw, o_bw, p, onw):
    t = o_fw.shape[0]

    def body(of_ref, ob_ref, g1_ref, g2_ref, w_ref, y_ref):
        for h in range(NH):
            sl = slice(h * DH, (h + 1) * DH)
            o = of_ref[:, sl] + ob_ref[:, sl]
            g_ref = g1_ref if h < NH // 2 else g2_ref
            gs = slice((h % (NH // 2)) * DH, (h % (NH // 2) + 1) * DH)
            n = o * _rstd(o) * w_ref[:, sl]
            y_ref[:, sl] = (n * _silu(g_ref[:, gs].astype(F32))).astype(BF16)

    return pl.pallas_call(
        body, name="post_fwd", grid=(t // TR,),
        in_specs=[_row(D), _row(D), _rowcol(HW, MAIN0 // HW + 4), _rowcol(HW, MAIN0 // HW + 8), _full((1, D))],
        out_specs=_row(D),
        out_shape=jax.ShapeDtypeStruct((t, D), BF16),
        compiler_params=_cp("parallel"),
    )(o_fw, o_bw, p, p, onw)


def _gate_window_specs(col0):
    return [_rowcol(HW, col0 // HW), _rowcol(HW, col0 // HW + 1), _rowcol(DH, (col0 + 2 * HW) // DH)]


def _gate_window(refs):
    return jnp.concatenate([r[...].astype(F32) for r in refs], axis=1)


def _branch_merge(y, w_hg, w_gla, p):
    t = y.shape[0]

    def body(y_ref, wh_ref, wg_ref, a0, a1, a2, b0, b1, b2, u1_ref, u2_ref, m_ref):
        u1 = _dot(y_ref[:, :HW], wh_ref[...], NN)
        u2 = _dot(y_ref[:, HW:], wg_ref[...], NN)
        u1_ref[...] = u1.astype(BF16)
        u2_ref[...] = u2.astype(BF16)
        m_ref[...] = (_sig(_gate_window((a0, a1, a2))) * u1 + _sig(_gate_window((b0, b1, b2))) * u2).astype(BF16)

    out = jax.ShapeDtypeStruct((t, GW), BF16)
    return pl.pallas_call(
        body, name="branch_merge", grid=(t // TR,),
        in_specs=[_row(D), _full((HW, GW)), _full((HW, GW))] + _gate_window_specs(GATE_HG0)
        + _gate_window_specs(GATE_GLA0),
        out_specs=[_row(GW)] * 3, out_shape=[out] * 3,
        compiler_params=_cp("parallel"),
    )(y, w_hg, w_gla, p, p, p, p, p, p)


def _mid_fwd(z, y1, nw_post, nw_pre, modc, modx):
    t = y1.shape[0]

    def body(zc_ref, zx_ref, y_ref, wpo_ref, wpr_ref, mc_ref, mx_ref, z1_ref, h_ref):
        is_ctx = pl.program_id(0) < NCT
        y = y_ref[...].astype(F32)
        z1 = _z_tile(zc_ref, zx_ref, is_ctx) + _mod_row(mc_ref, mx_ref, 2, is_ctx) * (y * _rstd(y) * wpo_ref[...])
        z1_ref[...] = z1
        n = z1 * _rstd(z1) * wpr_ref[...]
        h = n * (1.0 + _mod_row(mc_ref, mx_ref, 4, is_ctx)) + _mod_row(mc_ref, mx_ref, 3, is_ctx)
        h_ref[...] = h.astype(BF16)

    return pl.pallas_call(
        body, name="mid_fwd", grid=(t // TR,),
        in_specs=_z_specs() + [_row(D), _full((1, D)), _full((1, D)), _full((8, D)), _full((8, D))],
        out_specs=[_row(D), _row(D)],
        out_shape=[jax.ShapeDtypeStruct((t, D), F32), jax.ShapeDtypeStruct((t, D), BF16)],
        compiler_params=_cp("parallel"),
    )(*z, y1, nw_post, nw_pre, modc, modx)


def _final(z1, y2, target, nw, modc, modx):
    t = z1.shape[0]

    def body(z1_ref, y_ref, tg_ref, w_ref, mc_ref, mx_ref, dz_ref, dy_ref, loss_ref, sm_ref):
        i = pl.program_id(0)
        is_ctx = i < NCT

        @pl.when(i == 0)
        def _():
            loss_ref[...] = jnp.zeros_like(loss_ref)
            sm_ref[...] = jnp.zeros_like(sm_ref)

        g = _mod_row(mc_ref, mx_ref, 5, is_ctx)
        y = y_ref[...].astype(F32)
        r = _rstd(y)
        w = w_ref[...]
        yr = y * r
        n = yr * w
        e = z1_ref[...] + g * n - tg_ref[...]
        lat = jnp.where(is_ctx, 0.0, 1.0)
        loss_ref[...] += lat * _colsum(e * e)
        dz = e * (lat / D)
        dz_ref[...] = dz
        _acc_mod(sm_ref, 0, is_ctx, _colsum(dz * n))
        dn = dz * g
        _acc_row(sm_ref, 2, _colsum(dn * yr))
        dy_ref[...] = _rms_bwd(dn * w, y, r).astype(BF16)

    return pl.pallas_call(
        body, name="final", grid=(t // TR,),
        in_specs=[_row(D), _row(D), pl.BlockSpec((TR, D), lambda i: (jnp.maximum(i - NCT, 0), 0)),
                  _full((1, D)), _full((8, D)), _full((8, D))],
        out_specs=[_row(D), _row(D), _full((1, D)), _full((8, D))],
        out_shape=[jax.ShapeDtypeStruct((t, D), F32), jax.ShapeDtypeStruct((t, D), BF16),
                   jax.ShapeDtypeStruct((1, D), F32), jax.ShapeDtypeStruct((8, D), F32)],
        compiler_params=_cp("arbitrary"),
    )(z1, y2, target, nw, modc, modx)


def _mid_bwd(dh2, dz, z1, y1, nw_post, nw_pre, modc, modx):
    t = z1.shape[0]

    def body(dh_ref, dz_ref, z1_ref, y_ref, wpo_ref, wpr_ref, mc_ref, mx_ref, dzo_ref, dy_ref, sm_ref):
        i = pl.program_id(0)
        is_ctx = i < NCT

        @pl.when(i == 0)
        def _():
            sm_ref[...] = jnp.zeros_like(sm_ref)

        dh = dh_ref[...].astype(F32)
        z1 = z1_ref[...]
        r = _rstd(z1)
        zr = z1 * r
        wpr = wpr_ref[...]
        n = zr * wpr
        _acc_mod(sm_ref, 0, is_ctx, _colsum(dh))
        _acc_mod(sm_ref, 2, is_ctx, _colsum(dh * n))
        dn = dh * (1.0 + _mod_row(mc_ref, mx_ref, 4, is_ctx))
        _acc_row(sm_ref, 6, _colsum(dn * zr))
        dz1 = dz_ref[...] + _rms_bwd(dn * wpr, z1, r)
        dzo_ref[...] = dz1
        y = y_ref[...].astype(F32)
        r1 = _rstd(y)
        yr = y * r1
        wpo = wpo_ref[...]
        g = _mod_row(mc_ref, mx_ref, 2, is_ctx)
        _acc_mod(sm_ref, 4, is_ctx, _colsum(dz1 * (yr * wpo)))
        dn1 = dz1 * g
        _acc_row(sm_ref, 7, _colsum(dn1 * yr))
        dy_ref[...] = _rms_bwd(dn1 * wpo, y, r1).astype(BF16)

    return pl.pallas_call(
        body, name="mid_bwd", grid=(t // TR,),
        in_specs=[_row(D)] * 4 + [_full((1, D)), _full((1, D)), _full((8, D)), _full((8, D))],
        out_specs=[_row(D), _row(D), _full((8, D))],
        out_shape=[jax.ShapeDtypeStruct((t, D), F32), jax.ShapeDtypeStruct((t, D), BF16),
                   jax.ShapeDtypeStruct((8, D), F32)],
        compiler_params=_cp("arbitrary"),
    )(dh2, dz, z1, y1, nw_post, nw_pre, modc, modx)


def _pre_bwd(dh1, dz, z, nw, modc, modx):
    t = dh1.shape[0]

    def body(dh_ref, dz_ref, zc_ref, zx_ref, w_ref, mc_ref, mx_ref, dzo_ref, sm_ref):
        i = pl.program_id(0)
        is_ctx = i < NCT

        @pl.when(i == 0)
        def _():
            sm_ref[...] = jnp.zeros_like(sm_ref)

        dh = dh_ref[...].astype(F32)
        x = _z_tile(zc_ref, zx_ref, is_ctx)
        r = _rstd(x)
        xr = x * r
        w = w_ref[...]
        _acc_mod(sm_ref, 0, is_ctx, _colsum(dh))
        _acc_mod(sm_ref, 2, is_ctx, _colsum(dh * (xr * w)))
        dn = dh * (1.0 + _mod_row(mc_ref, mx_ref, 1, is_ctx))
        _acc_row(sm_ref, 4, _colsum(dn * xr))
        dzo_ref[...] = dz_ref[...] + _rms_bwd(dn * w, x, r)

    return pl.pallas_call(
        body, name="pre_bwd", grid=(t // TR,),
        in_specs=[_row(D)] * 2 + _z_specs() + [_full((1, D)), _full((8, D)), _full((8, D))],
        out_specs=[pl.BlockSpec((TR, D), lambda i: (jnp.maximum(i - NCT, 0), 0)), _full((8, D))],
        out_shape=[jax.ShapeDtypeStruct((t - CTX, D), F32), jax.ShapeDtypeStruct((8, D), F32)],
        compiler_params=_cp("arbitrary"),
    )(dh1, dz, *z, nw, modc, modx)


def _branch_merge_bwd(dm, p, u1, u2, w_hg, w_gla):
    t = dm.shape[0]

    def body(dm_ref, a0, a1, a2, b0, b1, b2, u1_ref, u2_ref, wh_ref, wg_ref, du1_ref, du2_ref, dg_ref, dyh_ref, dyg_ref):
        dm_ = dm_ref[...].astype(F32)
        s1 = _sig(_gate_window((a0, a1, a2)))
        s2 = _sig(_gate_window((b0, b1, b2)))
        du1 = (dm_ * s1).astype(BF16)
        du2 = (dm_ * s2).astype(BF16)
        du1_ref[...] = du1
        du2_ref[...] = du2
        dg_ref[:, :GW] = (dm_ * u1_ref[...].astype(F32) * s1 * (1.0 - s1)).astype(BF16)
        dg_ref[:, GW:] = (dm_ * u2_ref[...].astype(F32) * s2 * (1.0 - s2)).astype(BF16)
        dyh_ref[...] = _dot(du1, wh_ref[...], NT).astype(BF16)
        dyg_ref[...] = _dot(du2, wg_ref[...], NT).astype(BF16)

    return pl.pallas_call(
        body, name="branch_merge_bwd", grid=(t // TR,),
        in_specs=[_row(GW)] + _gate_window_specs(GATE_HG0) + _gate_window_specs(GATE_GLA0)
        + [_row(GW), _row(GW), _full((HW, GW)), _full((HW, GW))],
        out_specs=[_row(GW), _row(GW), _row(2 * GW), _row(HW), _row(HW)],
        out_shape=[jax.ShapeDtypeStruct((t, GW), BF16), jax.ShapeDtypeStruct((t, GW), BF16),
                   jax.ShapeDtypeStruct((t, 2 * GW), BF16), jax.ShapeDtypeStruct((t, HW), BF16),
                   jax.ShapeDtypeStruct((t, HW), BF16)],
        compiler_params=_cp("parallel"),
    )(dm, p, p, p, p, p, p, u1, u2, w_hg, w_gla)


def _post_bwd(dy_hg, dy_gla, o_fw, o_bw, p, onw):
    t = o_fw.shape[0]

    def body(d1_ref, d2_ref, of_ref, ob_ref, g1_ref, g2_ref, w_ref, do_ref, dg_ref, sm_ref):
        @pl.when(pl.program_id(0) == 0)
        def _():
            sm_ref[...] = jnp.zeros_like(sm_ref)

        for h in range(NH):
            sl = slice(h * DH, (h + 1) * DH)
            gs = slice((h % (NH // 2)) * DH, (h % (NH // 2) + 1) * DH)
            g_ref, d_ref = (g1_ref, d1_ref) if h < NH // 2 else (g2_ref, d2_ref)
            o = of_ref[:, sl] + ob_ref[:, sl]
            r = _rstd(o)
            orr = o * r
            w = w_ref[:, sl]
            gt = g_ref[:, gs].astype(F32)
            dy = d_ref[:, gs].astype(F32)
            dg_ref[:, sl] = (dy * (orr * w) * _dsilu(gt)).astype(BF16)
            dn = dy * _silu(gt)
            sm_ref[0:1, sl] += _colsum(dn * orr)
            do_ref[:, sl] = _rms_bwd(dn * w, o, r)

    return pl.pallas_call(
        body, name="post_bwd", grid=(t // TR,),
        in_specs=[_row(HW), _row(HW), _row(D), _row(D), _rowcol(HW, MAIN0 // HW + 4), _rowcol(HW, MAIN0 // HW + 8),
                  _full((1, D))],
        out_specs=[_row(D), _row(D), _full((8, D))],
        out_shape=[jax.ShapeDtypeStruct((t, D), F32), jax.ShapeDtypeStruct((t, D), BF16),
                   jax.ShapeDtypeStruct((8, D), F32)],
        compiler_params=_cp("arbitrary"),
    )(dy_hg, dy_gla, o_fw, o_bw, p, p, onw)


def _gates_bwd(p, hg_lb, wgk, bgk, dgm, dgo, dq_f, dq_b, dv_f, dv_b, dk_f, dk_b, dg_f, dg_b):
    t = p.shape[0]
    seg = lambda j: _rowcol(HW, MAIN0 // HW + j)

    def body(hq_ref, hf_ref, hb_ref, lr_ref, lb_ref, wgk_ref, bgk_ref, dgm_ref, dgo_ref,
             dqf_ref, dqb_ref, dvf_ref, dvb_ref, dkf_ref, dkb_ref, dgf_ref, dgb_ref,
             dp_ref, dlb_ref, dw_ref, db_ref):
        @pl.when(pl.program_id(0) == 0)
        def _():
            dlb_ref[...] = jnp.zeros_like(dlb_ref)
            dw_ref[...] = jnp.zeros_like(dw_ref)
            db_ref[...] = jnp.zeros_like(db_ref)

        c0 = MAIN0

        def put(j, val):
            dp_ref[:, c0 + j * HW:c0 + (j + 1) * HW] = val.astype(BF16)

        dq = dqf_ref[...].astype(F32) + dqb_ref[...].astype(F32)
        dv = dvf_ref[...].astype(F32) + dvb_ref[...].astype(F32)
        put(0, dq[:, :HW] * _dsilu(hq_ref[...].astype(F32)))
        put(1, dv[:, :HW])
        put(5, dq[:, HW:] * (DH ** -0.5))
        put(7, dv[:, HW:])
        put(6, dkf_ref[:, HW:].astype(F32) + dkb_ref[:, HW:].astype(F32))
        dp_ref[:, c0 + 4 * HW:c0 + 5 * HW] = dgo_ref[:, :HW]
        dp_ref[:, c0 + 8 * HW:c0 + 9 * HW] = dgo_ref[:, HW:]
        lr = lr_ref[...].astype(BF16)
        xg = _dot(lr, wgk_ref[...], NN) + bgk_ref[...]
        dxg = []
        for d, (raw_ref, dk_ref, dg_ref) in enumerate(((hf_ref, dkf_ref, dgf_ref), (hb_ref, dkb_ref, dgb_ref))):
            lbd = _hg_lb(lb_ref, d)
            s = _sig(raw_ref[...].astype(F32))
            f = lbd + (1.0 - lbd) * s
            df = dg_ref[:, :HW] / f - dk_ref[:, :HW].astype(F32)
            put(2 + d, df * (1.0 - lbd) * s * (1.0 - s))
            dlb_ref[d:d + 1, :] += _colsum(df * (1.0 - s)) * (lbd * (1.0 - lbd))
            dxg.append(dg_ref[:, HW:] * (1.0 / GLA_NORM) * _sig(-xg[:, d * HW:(d + 1) * HW]))
        dxg = jnp.concatenate(dxg, axis=1)
        db_ref[0:1, :] += _colsum(dxg)
        dxg_b = dxg.astype(BF16)
        dw_ref[...] += _dot(lr, dxg_b, TN)
        dlr = _dot(dxg_b, wgk_ref[...], NT)
        dp_ref[:, LR0:LR0 + DH] = (dlr + dgm_ref[:, :DH].astype(F32)).astype(BF16)
        dp_ref[:, LR0 + DH:GATE_GLA0] = dgm_ref[:, DH:D]
        dp_ref[:, GATE_GLA0:GATE_GLA0 + DH] = dgm_ref[:, D:GW] + dgm_ref[:, GW:GW + DH]
        dp_ref[:, GATE_GLA0 + DH:GATE_GLA0 + GW] = dgm_ref[:, GW + DH:]
        dp_ref[:, GATE_GLA0 + GW:] = jnp.zeros((TR, W_IN_COLS - GATE_GLA0 - GW), BF16)

    return pl.pallas_call(
        body, name="gates_bwd", grid=(t // TR,),
        in_specs=[seg(0), seg(2), seg(3), _rowcol(DH, LR0 // DH), _full((2, 2, HW)), _full((DH, D)), _full((1, D)),
                  _row(2 * GW), _row(D)] + [_row(D)] * 8,
        out_specs=[_row(W_IN_COLS), _full((8, HW)), _full((DH, D)), _full((8, D))],
        out_shape=[jax.ShapeDtypeStruct((t, W_IN_COLS), BF16), jax.ShapeDtypeStruct((8, HW), F32),
                   jax.ShapeDtypeStruct((DH, D), F32), jax.ShapeDtypeStruct((8, D), F32)],
        compiler_params=_cp("arbitrary"),
    )(p, p, p, p, hg_lb, wgk, bgk, dgm, dgo, dq_f, dq_b, dv_f, dv_b, dk_f, dk_b, dg_f, dg_b)


def _scan_consts(rev):
    r = lax.broadcasted_iota(jnp.int32, (CHUNK, CHUNK), 0)
    u = lax.broadcasted_iota(jnp.int32, (CHUNK, CHUNK), 1)
    rp = lax.broadcasted_iota(jnp.int32, (CHUNK, 1), 0)
    if rev:
        r, u, rp = CHUNK - 1 - r, CHUNK - 1 - u, CHUNK - 1 - rp
    tri = jnp.where(u <= r, 1.0, 0.0).astype(F32)
    tri_t = jnp.where(r <= u, 1.0, 0.0).astype(F32)
    lv = []
    for b in LEVELS:
        sh = b.bit_length() - 1
        pair = ((r >> sh) == (u >> sh) + 1) & (((u >> sh) & 1) == 0)
        pair_t = ((u >> sh) == (r >> sh) + 1) & (((r >> sh) & 1) == 0)
        tside = ((rp >> sh) & 1) == 1
        lv.append((pair, pair_t, tside, jnp.where(tside, 1.0, -1.0).astype(F32)))
    bd = LEVELS[-1].bit_length() - 1
    diag = ((r >> bd) == (u >> bd)) & (u <= r)
    diag_t = ((r >> bd) == (u >> bd)) & (r <= u)
    return tri, tri_t, lv, diag, diag_t


def _row_of(pos, rev):
    return CHUNK - 1 - pos if rev else pos


def _chunk_terms(cum, b_scr, consts, rev):
    _, _, lv, _, _ = consts
    terms = []
    for b, (_, _, _, sgn) in zip(LEVELS, lv):
        pieces = []
        for j in range(CHUNK // (2 * b)):
            row = _row_of(2 * b * j + b - 1, rev)
            pieces.append(jnp.broadcast_to(b_scr[row:row + 1, :], (2 * b, DH)))
        if rev:
            pieces = pieces[::-1]
        bnd = pieces[0] if len(pieces) == 1 else jnp.concatenate(pieces, axis=0)
        terms.append(jnp.exp((cum - bnd) * sgn))
    b = LEVELS[-1]
    pieces = []
    for j in range(CHUNK // b):
        if j == 0:
            pieces.append(jnp.zeros((b, DH), F32))
        else:
            row = _row_of(b * j - 1, rev)
            pieces.append(jnp.broadcast_to(b_scr[row:row + 1, :], (b, DH)))
    if rev:
        pieces = pieces[::-1]
    start = jnp.concatenate(pieces, axis=0)
    wq = jnp.exp(jnp.minimum(cum - start, 0.0))
    wk = jnp.exp(jnp.minimum(start - cum, EXP_CLAMP))
    terms.append((wq, wk))
    return terms


def _run_staged(units):
    live = list(units)
    while live:
        nxt = []
        for u in live:
            try:
                next(u)
                nxt.append(u)
            except StopIteration:
                pass
        live = nxt


SCAN_TB = 256
SCAN_CB = SCAN_TB // CHUNK


def _block_order(i, ntb, rev):
    nctx = CTX // SCAN_TB
    if not rev:
        return i
    return jnp.where(i < nctx, nctx - 1 - i, ntb - 1 - (i - nctx))


def _chunk_in_block(j, rev):
    return SCAN_CB - 1 - j if rev else j


def _scan_fwd(q, k, v, g, rev):
    t = q.shape[0]
    nc = t // CHUNK
    hpb = SCAN_HEADS_FWD

    def body(q_ref, k_ref, v_ref, g_ref, o_ref, st_ref, s_scr, b_scr):
        consts = _scan_consts(rev)
        _, _, lv, diag, _ = consts
        masks = [lvl[0] for lvl in lv] + [diag]

        @pl.when(pl.program_id(1) == 0)
        def _():
            s_scr[...] = jnp.zeros_like(s_scr)

        tri = consts[0]
        state = {hh: s_scr[hh] for hh in range(hpb)}

        def unit(hh, j):
            sl = slice(hh * DH, (hh + 1) * DH)
            c = _chunk_in_block(j, rev)
            rows = slice(c * CHUNK, (c + 1) * CHUNK)
            b_ref = b_scr.at[hh * SCAN_CB + j]
            qc, kc, vc, gc = q_ref[rows, sl], k_ref[rows, sl], v_ref[rows, sl], g_ref[rows, sl]
            cum = _split_dot(tri, gc)
            b_ref[...] = cum
            yield
            terms = _chunk_terms(cum, b_ref, consts, rev)
            qf, kf = qc.astype(F32), kc.astype(F32)
            xs = [(jnp.where(tside, qf, kf) * w).astype(BF16) for w, (_, _, tside, _) in zip(terms[:-1], lv)]
            qd, kd = (qf * terms[-1][0]).astype(BF16), (kf * terms[-1][1]).astype(BF16)
            tot = _colsum(gc)
            qe = (qf * jnp.exp(cum)).astype(BF16)
            ke = (kf * jnp.exp(tot - cum)).astype(BF16)
            vb = vc.astype(BF16)
            yield
            scs = [_dot(x, x, NT) for x in xs] + [_dot(qd, kd, NT)]
            kv = _dot(vb, ke, TN)
            yield
            a = jnp.zeros((CHUNK, CHUNK), F32)
            for sc, m in zip(scs, masks):
                a = a + jnp.where(m, sc, 0.0)
            o_intra = _dot(a.astype(BF16), vb, NN)
            yield
            st = state[hh]
            st_ref[hh, c] = st
            o_ref[rows, sl] = o_intra + _dot(qe, st.astype(BF16), NT)
            state[hh] = st * jnp.exp(tot) + kv
            yield

        _run_staged([unit(hh, j) for hh in range(hpb) for j in range(SCAN_CB)])
        for hh in range(hpb):
            s_scr[hh] = state[hh]

    ntb = t // SCAN_TB
    col = pl.BlockSpec((SCAN_TB, hpb * DH), lambda h, i: (_block_order(i, ntb, rev), h))
    return pl.pallas_call(
        body, name="scan_fwd_" + ("bw" if rev else "fw"), grid=(NH // hpb, ntb),
        in_specs=[col] * 4,
        out_specs=[col, pl.BlockSpec((hpb, SCAN_CB, DH, DH), lambda h, i: (h, _block_order(i, ntb, rev), 0, 0))],
        out_shape=[jax.ShapeDtypeStruct((t, D), F32), jax.ShapeDtypeStruct((NH, nc, DH, DH), F32)],
        scratch_shapes=[pltpu.VMEM((hpb, DH, DH), F32), pltpu.VMEM((hpb * SCAN_CB, CHUNK, DH), F32)],
        compiler_params=_cp("parallel", "arbitrary"),
    )(q, k, v, g)


def _scan_bwd(q, k, v, g, do, states, rev):
    t = q.shape[0]
    nc = t // CHUNK
    hpb = SCAN_HEADS_BWD

    def body(q_ref, k_ref, v_ref, g_ref, do_ref, st_ref, dq_ref, dk_ref, dv_ref, dg_ref, ds_scr, b_scr):
        consts = _scan_consts(rev)
        _, tri_t, lv, diag, diag_t = consts
        masks = [(lvl[0], lvl[1]) for lvl in lv] + [(diag, diag_t)]
        @pl.when(pl.program_id(1) == 0)
        def _():
            ds_scr[...] = jnp.zeros_like(ds_scr)

        tri = consts[0]
        dstate = {hh: ds_scr[hh] for hh in range(hpb)}

        def unit(hh, jj):
            sl = slice(hh * DH, (hh + 1) * DH)
            c = _chunk_in_block(SCAN_CB - 1 - jj, rev)
            rows = slice(c * CHUNK, (c + 1) * CHUNK)
            b_ref = b_scr.at[hh * SCAN_CB + jj]
            qc, kc, vc, gc = q_ref[rows, sl], k_ref[rows, sl], v_ref[rows, sl], g_ref[rows, sl]
            dob = do_ref[rows, sl].astype(BF16)
            vb = vc.astype(BF16)
            cum = _split_dot(tri, gc)
            b_ref[...] = cum
            da = _dot(dob, vb, NT)
            da_t = _dot(vb, dob, NT)
            yield
            terms = _chunk_terms(cum, b_ref, consts, rev)
            qf, kf = qc.astype(F32), kc.astype(F32)
            xs = [(jnp.where(tside, qf, kf) * w).astype(BF16) for w, (_, _, tside, _) in zip(terms[:-1], lv)]
            wqd, wkd = terms[-1]
            qdb, kdb = (qf * wqd).astype(BF16), (kf * wkd).astype(BF16)
            tot = _colsum(gc)
            e_tot = jnp.exp(tot)
            e_b = jnp.exp(cum)
            e_t = jnp.exp(tot - cum)
            qeb = (qf * e_b).astype(BF16)
            keb = (kf * e_t).astype(BF16)
            dsym = [(jnp.where(m, da, 0.0) + jnp.where(m_t, da_t, 0.0)).astype(BF16) for m, m_t in masks[:-1]]
            dad = (jnp.where(diag, da, 0.0).astype(BF16), jnp.where(diag_t, da_t, 0.0).astype(BF16))
            yield
            sym = [_dot(x, x, NT) for x in xs]
            dxs = [_dot(d, x, NN) for d, x in zip(dsym, xs)]
            at_d = _dot(kdb, qdb, NT)
            dqt_d = _dot(dad[0], kdb, NN)
            dkt_d = _dot(dad[1], qdb, NN)
            qd = _dot(dob, qeb, TN)
            yield
            a_t = jnp.where(diag_t, at_d, 0.0)
            dq = dqt_d * wqd
            dk = dkt_d * wkd
            db = dqt_d * qdb.astype(F32) - dkt_d * kdb.astype(F32)
            for s, dx, x, w, (_, m_t, tside, sgn) in zip(sym, dxs, xs, terms[:-1], lv):
                a_t = a_t + jnp.where(m_t, s, 0.0)
                dxw = dx * w
                dq = dq + jnp.where(tside, dxw, 0.0)
                dk = dk + jnp.where(tside, 0.0, dxw)
                db = db + (dx * x.astype(F32)) * sgn
            dv_intra = _dot(a_t.astype(BF16), dob, NN)
            st = st_ref[hh, c]
            stb = st.astype(BF16)
            dqe = _dot(dob, stb, NN)
            yield
            dst = dstate[hh]
            dstb = dst.astype(BF16)
            dstate[hh] = dst * e_tot + qd
            dv_ref[rows, sl] = (dv_intra + _dot(keb, dstb, NT)).astype(BF16)
            dke = _dot(vb, dstb, NN)
            yield
            qe = qeb.astype(F32)
            ke = keb.astype(F32)
            dq_ref[rows, sl] = (dq + dqe * e_b).astype(BF16)
            dk_ref[rows, sl] = (dk + dke * e_t).astype(BF16)
            db = db + dqe * qe - dke * ke
            dtot = _colsum(dstb.astype(F32) * stb.astype(F32)) * e_tot + _colsum(dke * ke)
            dg_ref[rows, sl] = _split_dot(tri_t, db) + dtot
            yield

        _run_staged([unit(hh, jj) for hh in range(hpb) for jj in range(SCAN_CB)])
        for hh in range(hpb):
            ds_scr[hh] = dstate[hh]

    ntb = t // SCAN_TB
    blk = lambda i: _block_order(ntb - 1 - i, ntb, rev)
    col = pl.BlockSpec((SCAN_TB, hpb * DH), lambda h, i: (blk(i), h))
    out = jax.ShapeDtypeStruct((t, D), F32)
    outb = jax.ShapeDtypeStruct((t, D), BF16)
    return pl.pallas_call(
        body, name="scan_bwd_" + ("bw" if rev else "fw"), grid=(NH // hpb, ntb),
        in_specs=[col] * 5 + [pl.BlockSpec((hpb, SCAN_CB, DH, DH), lambda h, i: (h, blk(i), 0, 0))],
        out_specs=[col] * 4,
        out_shape=[outb] * 3 + [out],
        scratch_shapes=[pltpu.VMEM((hpb, DH, DH), F32), pltpu.VMEM((hpb * SCAN_CB, CHUNK, DH), F32)],
        compiler_params=_cp("parallel", "arbitrary"),
    )(q, k, v, g, do, states)


W_IN_GRAD_CHUNKS = (("a", (0, 512)), ("b", (0, 256)), ("b", (256, 512)))
W_IN_REF = 6688
W_IN_PAD = 896
W_IN_PIECE = 256
W_IN_STAGES = (3, 4)


def _assemble_w_in(g, rows, prev, name):
    n, r, wp = g.shape
    tr = W_IN_PIECE
    tiles = wp // DH
    first = rows[0] // tr

    def body(g_ref, *refs):
        o_ref = refs[-1]
        lane = lax.broadcasted_iota(jnp.int32, (tr, DH), 1)
        for t in range(W_IN_COLS // DH):
            acc = None
            for j in range(n):
                c = DH * t - W_IN_SHARD * j
                if c <= -DH or c >= W_IN_SHARD:
                    continue
                k, s = divmod(c, DH)
                lo = g_ref[j, :, k * DH:(k + 1) * DH] if 0 <= k < tiles else None
                hi = g_ref[j, :, (k + 1) * DH:(k + 2) * DH] if s and 0 <= k + 1 < tiles else None
                if s:
                    zero = jnp.zeros((tr, DH), g.dtype)
                    lo = zero if lo is None else pltpu.roll(lo, DH - s, 1)
                    hi = zero if hi is None else pltpu.roll(hi, DH - s, 1)
                    part = jnp.where(lane < DH - s, lo, hi)
                else:
                    part = lo
                acc = part if acc is None else acc + part
            o_ref[:, t * DH:(t + 1) * DH] = jnp.zeros((tr, DH), g.dtype) if acc is None else acc

    held = [] if prev is None else [prev]
    return pl.pallas_call(
        body, name=name, grid=((rows[1] - rows[0]) // tr,),
        in_specs=[pl.BlockSpec((n, tr, wp), lambda i: (0, first + i, 0))] + [pl.BlockSpec(memory_space=pl.ANY)] * len(held),
        out_specs=pl.BlockSpec((tr, W_IN_COLS), lambda i: (first + i, 0)),
        out_shape=jax.ShapeDtypeStruct((r, W_IN_COLS), g.dtype),
        input_output_aliases={1: 0} if held else {},
        compiler_params=_cp("parallel"),
    )(g, *held)


def _gate_cols(w):
    return jnp.pad(w, ((0, 0), (GOFF, GW - GOFF - D)))


def _gate_rows(w):
    return jnp.pad(w, ((GOFF, GW - GOFF - D), (0, 0)))


def _layout_wgk(w):
    r = w.shape[1]
    top = jnp.concatenate([w[0], jnp.zeros_like(w[0])], axis=1)
    bot = jnp.concatenate([jnp.zeros_like(w[1]), w[1]], axis=1)
    return jnp.concatenate([top, bot, jnp.zeros((DH - 2 * r, D), w.dtype)], axis=0)


def _unlayout_wgk(d, r=16):
    return jnp.stack([d[:r, :HW], d[r:2 * r, HW:]])


def _local_step(z, target, modc, modx, norms, onw, hg_lb, wgk, bgk, get_w_in, get_mix, get_ffn, send):
    n_pre1, n_post1, n_pre2, n_post2 = norms
    t = z[0].shape[0] + z[1].shape[0]
    tm = 1152 if t % 1152 == 0 else 256
    h1 = _prenorm(z, n_pre1, modc, modx, 0, 1, "prenorm1")
    w_in = get_w_in(h1)
    p = _matmul(h1, w_in, NN, BF16, "mm_in", t, 1024, D)
    q, v, k_f, k_b, g_f, g_b = _gates_fwd(p, hg_lb, wgk, bgk)
    o_f, st_f = _scan_fwd(q, k_f, v, g_f, False)
    o_b, st_b = _scan_fwd(q, k_b, v, g_b, True)
    y = _post_fwd(o_f, o_b, p, onw)
    w_br_hg, w_br_gla, w_out = get_mix(y)
    u1, u2, merged = _branch_merge(y, w_br_hg, w_br_gla, p)
    y1 = _matmul(merged, w_out, NN, BF16, "mm_out", tm, 512, GW)
    z1, h2 = _mid_fwd(z, y1, n_post1, n_pre2, modc, modx)
    w_gu_t, w_down = get_ffn(h2)
    u, v_ff, act = _mm_gu_act(h2, w_gu_t[0], w_gu_t[1], "mm_gu", tm)
    y2 = _matmul(act, w_down, NN, BF16, "mm_down", t, 512, D_FF)
    dz, dy2, loss_vec, sm_final = _final(z1, y2, target, n_post2, modc, modx)
    du, dv_ff = _mm_down_dx_act(dy2, w_down, u, v_ff, "mm_down_dx", tm)
    d_w_down = _matmul(act, dy2, TN, BF16, "mm_down_dw", D_FF // 2, 1024, t)
    dh2 = _matmul((du, dv_ff), w_gu_t, NN, BF16, "mm_gu_dx", tm, 512, D_FF)
    d_w_gate_t = _matmul(du, h2, TN, BF16, "mm_gate_dw", D_FF // 2, 1024, t)
    d_w_up_t = _matmul(dv_ff, h2, TN, BF16, "mm_up_dw", D_FF // 2, 1024, t)
    dh2 = send(("w_down", "w_gate_t", "w_up_t"), (d_w_down, d_w_gate_t, d_w_up_t), dh2)
    dz, dy1, sm_mid = _mid_bwd(dh2, dz, z1, y1, n_post1, n_pre2, modc, modx)
    dmerged = _matmul(dy1, w_out, NT, BF16, "mm_out_dx", tm, GW, D)
    d_w_out = _matmul(merged, dy1, TN, BF16, "mm_out_dw", GW, 512, t)
    du1, du2, dgm, dy_hg, dy_gla = _branch_merge_bwd(dmerged, p, u1, u2, w_br_hg, w_br_gla)
    d_w_br_hg = _matmul(y, du1, TN, BF16, "mm_br_hg_dw", HW, GW, t, a_off=0, m_out=HW)
    d_w_br_gla = _matmul(y, du2, TN, BF16, "mm_br_gla_dw", HW, GW, t, a_off=1, m_out=HW)
    dy_hg = send(("w_out", "w_br_hg", "w_br_gla"), (d_w_out, d_w_br_hg, d_w_br_gla), dy_hg)
    do, dgo, sm_post = _post_bwd(dy_hg, dy_gla, o_f, o_b, p, onw)
    dq_f, dk_f, dv_f, dg_f = _scan_bwd(q, k_f, v, g_f, do, st_f, False)
    dq_b, dk_b, dv_b, dg_b = _scan_bwd(q, k_b, v, g_b, do, st_b, True)
    dp, d_lb, d_wgk, d_bgk = _gates_bwd(p, hg_lb, wgk, bgk, dgm, dgo, dq_f, dq_b, dv_f, dv_b, dk_f, dk_b, dg_f, dg_b)
    d_w_in_a = _matmul(h1, dp, TN, BF16, "mm_in_dw_a", 512, 1024, t, a_off=0, m_out=D // 2)
    dp = send(("w_in_a",), (d_w_in_a,), dp)
    d_w_in_b = _matmul(h1, dp, TN, BF16, "mm_in_dw_b", 512, 1024, t, a_off=1, m_out=D // 2)
    dp = send(("w_in_b",), (d_w_in_b,), dp)
    dh1 = _matmul(dp, w_in, NT, BF16, "mm_in_dx", tm, 512, W_IN_COLS // 2)
    grad_x, sm_pre = _pre_bwd(dh1, dz, z, n_pre1, modc, modx)
    return dict(loss_vec=loss_vec, grad_x=grad_x, sm_final=sm_final, sm_mid=sm_mid, sm_post=sm_post, sm_pre=sm_pre,
                d_lb=d_lb, d_wgk=d_wgk, d_bgk=d_bgk)


MESH = pl.DeviceIdType.MESH
ANY = pl.BlockSpec(memory_space=pl.ANY)
N_REL = N_DEV - 1


def _place():
    return lax.axis_index("x"), lax.axis_index("y"), lax.axis_index("c")


def _slot(p):
    return 4 * p[0] + 2 * p[1] + p[2]


HBM = pl.BlockSpec(memory_space=pltpu.HBM)
SEM = pl.BlockSpec(memory_space=pltpu.SEMAPHORE)
EFFECT = pltpu.SideEffectType.DATAFLOW_SIDE_EFFECTING


def _peer_of(x, y, c, k):
    flip = lambda v, bit: 1 - v if bit else v
    return flip(x, k & 4), flip(y, k & 2), flip(c, k & 1)


def _view_whole(src, slot):
    return src


def _view_near(src, slot):
    return src


_view_near.peers = (1, 2, 4, 6)


def _view_near_rows(rows):
    def view(src, slot):
        return src.at[pl.ds(rows[0], rows[1] - rows[0])]
    view.peers = _view_near.peers
    view.land = lambda land, slot: land.at[slot, pl.ds(rows[0], rows[1] - rows[0])]
    return view


def _view_block(src, slot):
    return src.at[slot]


W_IN_SHARD = W_IN_REF // N_DEV


def _view_window(rows):
    def view(src, slot):
        col0 = pl.multiple_of((W_IN_SHARD * slot // DH) * DH, DH)
        return src.at[pl.ds(rows[0], rows[1] - rows[0]), pl.ds(col0, D)]
    return view


def _split_copies(view, srcs, lands, send_sems, recv_sems, local_sems):
    x, y, c = _place()
    me = _slot((x, y, c))
    into = getattr(view, "land", lambda land, slot: land.at[slot])
    local, sends, waits = [], [], []
    for a, (src, land) in enumerate(zip(srcs, lands)):
        local.append(pltpu.make_async_copy(view(src, me), into(land, me), local_sems.at[a]))
        for k in getattr(view, "peers", range(1, N_DEV)):
            peer = _peer_of(x, y, c, k)
            mine = view(src, _slot(peer))
            sems = dict(send_sem=send_sems.at[N_REL * a + k - 1], recv_sem=recv_sems.at[N_REL * a + k - 1],
                        device_id=peer, device_id_type=MESH)
            sends.append(pltpu.make_async_remote_copy(src_ref=mine, dst_ref=into(land, me), **sems))
            waits.append(pltpu.make_async_remote_copy(src_ref=mine, dst_ref=into(land, _slot(peer)), **sems))
    return local, sends, waits


def _split_start(groups, name, after):
    built = []
    for view, srcs, lands in groups:
        lands = [lax.empty(l, s.dtype) if isinstance(l, tuple) else l for l, s in zip(lands, srcs)]
        built.append((view, list(srcs), lands))
    bufs = [b for _, srcs, lands in built for b in srcs + lands]
    nb, ng = len(bufs), len(built)

    def body(*refs):
        buf_refs, sem_refs, token = refs[:nb], refs[nb + 1:nb + 1 + 3 * ng], refs[-1]
        pos = 0
        for i, (view, srcs, _) in enumerate(built):
            n = len(srcs)
            local, sends, _ = _split_copies(view, buf_refs[pos:pos + n], buf_refs[pos + n:pos + 2 * n],
                                            *sem_refs[3 * i:3 * i + 3])
            pos += 2 * n
            for cp in local + sends:
                cp.start()
        token[...] = jnp.zeros_like(token)

    sems = []
    for _, srcs, _ in built:
        n = len(srcs)
        sems += [pltpu.SemaphoreType.DMA((N_REL * n,)), pltpu.SemaphoreType.DMA((N_REL * n,)),
                 pltpu.SemaphoreType.DMA((n,))]
    hbm = lambda a: pltpu.with_memory_space_constraint(a, pltpu.HBM)
    out = pl.pallas_call(
        body, name=name,
        out_shape=(*sems, *[pltpu.HBM(b.shape, b.dtype) for b in bufs], jax.ShapeDtypeStruct((8, DH), F32)),
        in_specs=[HBM] * nb + [ANY],
        out_specs=(*([SEM] * (3 * ng)), *([HBM] * nb), pl.BlockSpec(memory_space=pltpu.VMEM)),
        input_output_aliases={i: 3 * ng + i for i in range(nb)},
        compiler_params=pltpu.CompilerParams(has_side_effects=EFFECT),
    )(*[hbm(b) for b in bufs], after)
    handles, pos = [], 3 * ng
    for i, (view, srcs, _) in enumerate(built):
        n = len(srcs)
        handles.append(dict(view=view, n=n, sems=out[3 * i:3 * i + 3], srcs=list(out[pos:pos + n]),
                            lands=list(out[pos + n:pos + 2 * n])))
        pos += 2 * n
    return handles, out[-1]


def _split_wait(handle, name, after, srcs=None, lands=None):
    view, n, sems = handle["view"], handle["n"], handle["sems"]
    srcs = handle["srcs"] if srcs is None else srcs
    lands = handle["lands"] if lands is None else lands
    afters = list(after) if isinstance(after, (list, tuple)) else [after]

    def body(*refs):
        src_refs, land_refs = refs[:n], refs[n:2 * n]
        send_sems, recv_sems, local_sems = refs[2 * n:2 * n + 3]
        local, _, waits = _split_copies(view, src_refs, land_refs, send_sems, recv_sems, local_sems)
        for cp in waits:
            cp.wait_send()
            cp.wait_recv()
        for cp in local:
            cp.wait()

    out = pl.pallas_call(
        body, name=name,
        out_shape=(*[pltpu.HBM(s.shape, s.dtype) for s in srcs], *[pltpu.HBM(l.shape, l.dtype) for l in lands]),
        in_specs=[HBM] * (2 * n) + [SEM, SEM, SEM] + [ANY] * len(afters),
        out_specs=tuple([HBM] * (2 * n)),
        input_output_aliases={i: i for i in range(2 * n)},
        compiler_params=pltpu.CompilerParams(has_side_effects=EFFECT),
    )(*srcs, *lands, *sems, *afters)
    handle["srcs"] = list(out[:n])
    return list(out[n:])


def _tie(x, token, name):
    def body(x_ref, t_ref, o_ref):
        pass

    return pl.pallas_call(
        body, name=name, out_shape=jax.ShapeDtypeStruct(x.shape, x.dtype),
        in_specs=[ANY, ANY], out_specs=ANY, input_output_aliases={0: 0},
    )(x, token)


def _forward_to_sibling(land, name, rows):
    def body(land_ref, out_ref, send_sems, recv_sems):
        x, y, c = _place()
        sibling = (x, y, 1 - c)
        chips = [(1 - x, y), (x, 1 - y), (1 - x, 1 - y)]
        piece = pl.ds(rows[0], rows[1] - rows[0])

        def copy(j, core):
            blk = _slot((*chips[j], core))
            return pltpu.make_async_remote_copy(src_ref=land_ref.at[blk, piece], dst_ref=out_ref.at[blk, piece],
                                                send_sem=send_sems.at[j], recv_sem=recv_sems.at[j],
                                                device_id=sibling, device_id_type=MESH)

        sends = [copy(j, c) for j in range(3)]
        for cp in sends:
            cp.start()
        for j in range(3):
            copy(j, 1 - c).wait_recv()
        for cp in sends:
            cp.wait_send()

    return pl.pallas_call(
        body, name=name, in_specs=[ANY], out_specs=ANY, input_output_aliases={0: 0},
        out_shape=jax.ShapeDtypeStruct(land.shape, land.dtype),
        scratch_shapes=[pltpu.SemaphoreType.DMA((3,)), pltpu.SemaphoreType.DMA((3,))],
    )(land)


def _mod_fwd(a, w, b):
    def body(a_ref, w_ref, b_ref, o_ref):
        o_ref[...] = _dot(_silu(a_ref[...]), w_ref[...], NN, precision=HI) + b_ref[...]

    return pl.pallas_call(
        body, name="mod_fwd", out_shape=jax.ShapeDtypeStruct((a.shape[0], w.shape[1]), F32),
        compiler_params=pltpu.CompilerParams(vmem_limit_bytes=VMEM_LIMIT),
    )(a, w, b)


def _mod_bwd(a, d, w):
    def body(a_ref, d_ref, w_ref, dw_ref, dc_ref):
        av = a_ref[...]
        dv = d_ref[...]
        dw_ref[...] = _dot(_silu(av), dv, TN, precision=HI)
        da = _dot(dv[0:8, :], w_ref[...], NT, precision=HI) * _dsilu(av[0:8, :])
        row = lax.broadcasted_iota(jnp.int32, da.shape, 0)
        dc_ref[...] = jnp.where(row == 0, da, 0.0)

    return pl.pallas_call(
        body, name="mod_bwd",
        out_shape=[jax.ShapeDtypeStruct(w.shape, F32), jax.ShapeDtypeStruct((8, w.shape[0]), F32)],
        compiler_params=pltpu.CompilerParams(vmem_limit_bytes=VMEM_LIMIT),
    )(a, d, w)


def _sum_devices(g):
    def body(g_ref, o_ref):
        acc = g_ref[0]
        for i in range(1, g.shape[0]):
            acc = acc + g_ref[i]
        o_ref[...] = acc

    return pl.pallas_call(body, name="sum_devices_%d" % g.shape[1],
                          out_shape=jax.ShapeDtypeStruct(g.shape[1:], F32))(g)


def _sum_windows(g, name):
    n, r, c = g.shape
    tr = 128

    def body(g_ref, o_ref):
        x, y, cc = _place()
        lane0 = (W_IN_SHARD * _slot((x, y, cc))) % DH
        acc = g_ref[0].astype(F32)
        for i in range(1, n):
            acc = acc + g_ref[i].astype(F32)
        o_ref[...] = pltpu.roll(acc, (c - lane0) % c, 1).T

    return pl.pallas_call(
        body, name=name, grid=(r // tr,),
        in_specs=[pl.BlockSpec((n, tr, c), lambda i: (0, i, 0))],
        out_specs=pl.BlockSpec((c, tr), lambda i: (0, i)),
        out_shape=jax.ShapeDtypeStruct((c, r), F32),
        compiler_params=_cp("parallel"),
    )(g)


def _adam_rows(r, c, n):
    budget = 6 * 1024 * 1024
    best = None
    for tr in range(16, r + 1, 16):
        if r % tr == 0 and tr * c * (2 * n + 28) <= budget:
            best = tr
    return best if best is not None else r


def _adamw(g, w, m, v, name):
    n, r, c = g.shape
    tr = _adam_rows(r, c, n)
    bc1 = 1.0 - ADAM_B1 ** ADAM_STEP
    bc2 = 1.0 - ADAM_B2 ** ADAM_STEP

    def body(g_ref, w_ref, m_ref, v_ref, go_ref, d_ref, mo_ref, vo_ref):
        grad = g_ref[0].astype(F32)
        for i in range(1, n):
            grad = grad + g_ref[i].astype(F32)
        go_ref[...] = grad
        m_new = ADAM_B1 * m_ref[...] + (1.0 - ADAM_B1) * grad
        v_new = ADAM_B2 * v_ref[...] + (1.0 - ADAM_B2) * (grad * grad)
        mo_ref[...] = m_new
        vo_ref[...] = v_new
        d_ref[...] = -ADAM_LR * ((m_new / bc1) / (jnp.sqrt(v_new / bc2) + ADAM_EPS) + ADAM_WD * w_ref[...])

    blk = pl.BlockSpec((tr, c), lambda i: (i, 0))
    out = jax.ShapeDtypeStruct((r, c), F32)
    return pl.pallas_call(
        body, name=name, grid=(r // tr,),
        in_specs=[pl.BlockSpec((n, tr, c), lambda i: (0, i, 0)), blk, blk, blk],
        out_specs=[blk] * 4, out_shape=[out] * 4,
        compiler_params=_cp("parallel"),
    )(g, w, m, v)


ADAM_ROWS3 = 168


def _adam_math(grad, w, m, v):
    bc1 = 1.0 - ADAM_B1 ** ADAM_STEP
    bc2 = 1.0 - ADAM_B2 ** ADAM_STEP
    m_new = ADAM_B1 * m + (1.0 - ADAM_B1) * grad
    v_new = ADAM_B2 * v + (1.0 - ADAM_B2) * (grad * grad)
    delta = -ADAM_LR * ((m_new / bc1) / (jnp.sqrt(v_new / bc2) + ADAM_EPS) + ADAM_WD * w)
    return delta, m_new, v_new


def _adamw_rows3(g, w3, m3, v3, name):
    r, _, c = w3.shape
    n = ADAM_ROWS3
    starts = list(range(0, r - n, n)) + [r - n]

    def body(g_hbm, w_hbm, m_hbm, v_hbm, go_hbm, d_hbm, mo_hbm, vo_hbm, gbuf, ibuf, obuf, in_sems, out_sems):
        def fetch(p):
            r0, slot = starts[p], p % 2
            g0 = (r0 // 8) * 8
            cps = [pltpu.make_async_copy(g_hbm.at[pl.ds(g0, n + 8)], gbuf.at[slot], in_sems.at[slot, 0])]
            cps += [pltpu.make_async_copy(h.at[pl.ds(r0, n), 0], ibuf.at[slot, k], in_sems.at[slot, 1 + k])
                    for k, h in enumerate((w_hbm, m_hbm, v_hbm))]
            for cp in cps:
                cp.start()
            return cps

        pending, outs = fetch(0), []
        for p, r0 in enumerate(starts):
            slot = p % 2
            nxt = fetch(p + 1) if p + 1 < len(starts) else []
            for cp in pending:
                cp.wait()
            grad = gbuf[slot, pl.ds(r0 - (r0 // 8) * 8, n), :]
            delta, m_new, v_new = _adam_math(grad, ibuf[slot, 0], ibuf[slot, 1], ibuf[slot, 2])
            for cp in outs:
                cp.wait()
            for k, val in enumerate((grad, delta, m_new, v_new)):
                obuf[slot, k] = val
            outs = [pltpu.make_async_copy(obuf.at[slot, k], h.at[pl.ds(r0, n), 0], out_sems.at[slot, k])
                    for k, h in enumerate((go_hbm, d_hbm, mo_hbm, vo_hbm))]
            for cp in outs:
                cp.start()
            pending = nxt
        for cp in outs:
            cp.wait()

    out = jax.ShapeDtypeStruct(w3.shape, F32)
    return pl.pallas_call(
        body, name=name, in_specs=[ANY] * 4, out_specs=[ANY] * 4, out_shape=[out] * 4,
        scratch_shapes=[pltpu.VMEM((2, n + 8, c), F32), pltpu.VMEM((2, 3, n, c), F32), pltpu.VMEM((2, 4, n, c), F32),
                        pltpu.SemaphoreType.DMA((2, 4)), pltpu.SemaphoreType.DMA((2, 4))],
        compiler_params=pltpu.CompilerParams(vmem_limit_bytes=VMEM_LIMIT),
    )(g, w3, m3, v3)


def kernel(x, c, ctx, c_ctx, w_mod, b_mod, norm_pre1, norm_post1, norm_pre2, norm_post2, w_in, hg_lb, hg_onorm, gla_w_gk, gla_b_gk, gla_onorm, w_br_hg, w_br_gla, w_out, w_ff_gate, w_ff_up, w_ff_down, loss_target, m_c_ctx, m_w_mod, m_b_mod, m_norm_pre1, m_norm_post1, m_norm_pre2, m_norm_post2, m_w_in, m_hg_lb, m_hg_onorm, m_gla_w_gk, m_gla_b_gk, m_gla_onorm, m_w_br_hg, m_w_br_gla, m_w_out, m_w_ff_gate, m_w_ff_up, m_w_ff_down, v_c_ctx, v_w_mod, v_b_mod, v_norm_pre1, v_norm_post1, v_norm_pre2, v_norm_post2, v_w_in, v_hg_lb, v_hg_onorm, v_gla_w_gk, v_gla_b_gk, v_gla_onorm, v_w_br_hg, v_w_br_gla, v_w_out, v_w_ff_gate, v_w_ff_up, v_w_ff_down):
    xi, yi, ci = lax.axis_index("x"), lax.axis_index("y"), lax.axis_index("c")
    me = 4 * xi + 2 * yi + ci
    t = CTX + x.shape[1]

    w_in_pieces, w_in_state = [], {}

    def w_in_piece(i):
        return (_view_near_rows((i * W_IN_PIECE, (i + 1) * W_IN_PIECE)), w_in_state["src"], w_in_state["land"])

    def started_w_in(handle):
        w_in_state.update(src=handle["srcs"], land=handle["lands"])
        w_in_pieces.append(handle)

    tr_ = lambda a: jnp.swapaxes(a[0], 0, 1)
    w_in_bf = jnp.pad(w_in[0].astype(BF16), ((0, 0), (0, W_IN_PAD - W_IN_SHARD)))
    w_in_state.update(src=[w_in_bf], land=[lax.empty((N_DEV,) + w_in_bf.shape, BF16)])
    gathered = lambda arrs: [(N_DEV,) + a.shape for a in arrs]
    whole = lambda arrs: (_view_whole, arrs, gathered(arrs))
    small_in = [c, hg_lb, gla_w_gk[0], gla_b_gk[0]]
    (small_handle, piece), tok = _split_start([whole(small_in), w_in_piece(0)], "ag_small_start", c)
    started_w_in(piece)
    c_all, lb_g, wgk_g, bgk_g = _split_wait(small_handle, "ag_small_wait", tok)
    big = [w_in[0], w_br_hg[0], w_br_gla[0], w_out[0], tr_(w_ff_gate), tr_(w_ff_up), w_ff_down[0]]
    big_bf = [None] + [w.astype(BF16) for w in big[1:]]
    cols = lambda g: jnp.transpose(g, (1, 0, 2)).reshape(g.shape[1], N_DEV * g.shape[2])

    def get_w_in(after):
        w_full, first = None, 0
        for s, last in enumerate(W_IN_STAGES):
            for i in range(first, last):
                land = _split_wait(w_in_pieces[i], "ag_w_in_wait%d" % i, after if w_full is None else [after, w_full],
                                   srcs=w_in_state["src"], lands=w_in_state["land"])
                w_in_state.update(src=w_in_pieces[i]["srcs"], land=land)
            rows = (first * W_IN_PIECE, last * W_IN_PIECE)
            w_in_state["land"] = [_forward_to_sibling(w_in_state["land"][0], "ag_w_in_forward%d" % s, rows)]
            w_full = _assemble_w_in(w_in_state["land"][0], rows, w_full, "assemble_w_in%d" % s)
            first = last
        return w_full

    def get_mix(after):
        g_brh, g_brg, g_out = _split_wait(mix_handle, "ag_mix_wait", after)
        return _gate_cols(cols(g_brh)), _gate_cols(cols(g_brg)), _gate_rows(g_out.reshape(D, D))

    def get_ffn(after):
        g_gate, g_up, g_down = _split_wait(ffn_handle, "ag_ffn_wait", after)
        return (g_gate.reshape(D_FF, D), g_up.reshape(D_FF, D)), g_down.reshape(D_FF, D)

    hg_lb_full = jnp.transpose(lb_g, (1, 2, 0, 3)).reshape(2, 2, HW)
    wgk_k = _layout_wgk(jnp.transpose(wgk_g, (1, 2, 0, 3)).reshape(2, 16, HW)).astype(BF16)
    bgk_k = jnp.transpose(bgk_g, (1, 0, 2)).reshape(1, D)
    onw = jnp.concatenate([jnp.tile(hg_onorm, (1, NH // 2)), jnp.tile(gla_onorm, (1, NH // 2))], axis=1)

    n_mod = w_mod.shape[2]
    a9 = jnp.concatenate([c_ctx[None], c_all[:, 0], jnp.zeros((16 - 1 - N_DEV, D), F32)], axis=0)
    b_loc = lax.dynamic_slice(b_mod, (0, me * n_mod), (1, n_mod))
    s_loc = _mod_fwd(a9, w_mod[0], b_loc)
    (mod_handle, piece), tok = _split_start([whole([s_loc]), w_in_piece(1)], "ag_mod_start", s_loc)
    started_w_in(piece)
    for i in range(2, D // W_IN_PIECE):
        (piece,), tok = _split_start([w_in_piece(i)], "ag_w_in_start%d" % i, tok)
        started_w_in(piece)
    s_all, = _split_wait(mod_handle, "ag_mod_wait", tok)
    mod_all = jnp.transpose(s_all, (1, 0, 2)).reshape(16, N_DEV * n_mod)
    pad8 = lambda m: jnp.concatenate([m.reshape(6, D), jnp.zeros((2, D), F32)], axis=0)
    modc = pad8(mod_all[0])
    modx = pad8(lax.dynamic_slice(mod_all, (1 + me, 0), (1, N_DEV * n_mod))[0])

    (mix_handle, ffn_handle), tok = _split_start([whole(big_bf[1:4]), whole(big_bf[4:])], "ag_big_start", s_all)

    z = (ctx[0], x[0])
    modx = _tie(modx, tok, "tie_mod")
    norms = (norm_pre1, norm_post1, norm_pre2, norm_post2)
    shard = lambda d: jnp.transpose(d.reshape(d.shape[0], N_DEV, -1), (1, 0, 2)).astype(BF16)
    rowshard = lambda d: d.reshape(N_DEV, d.shape[0] // N_DEV, d.shape[1]).astype(BF16)
    sent, w_in_grad = [], {}

    def w_in_chunk(i):
        half, rows = W_IN_GRAD_CHUNKS[i]
        return (_view_window(rows), w_in_grad[half], [(N_DEV, rows[1] - rows[0], D)])

    def sent_w_in(i, handle):
        w_in_grad[W_IN_GRAD_CHUNKS[i][0]] = handle["srcs"]
        sent.append(("w_in%d" % i, ["w_in#%d" % i], handle))

    def send(names, grads, x_after):
        if names == ("w_in_a",):
            w_in_grad["a"] = list(grads)
            (handle,), tok = _split_start([w_in_chunk(0)], "grads_w_in0_start", x_after)
            sent_w_in(0, handle)
            return _tie(x_after, tok, "tie_w_in0")
        if names == ("w_in_b",):
            w_in_grad["b"] = list(grads)
            return x_after
        arrs, leaves = [], []
        for nm, g in zip(names, grads):
            if nm in ("w_gate_t", "w_up_t"):
                arrs.append(rowshard(g))
                leaves.append({"w_gate_t": "w_ff_gate", "w_up_t": "w_ff_up"}[nm])
            elif nm == "w_down":
                arrs.append(rowshard(g))
                leaves.append("w_ff_down")
            elif nm == "w_out":
                arrs.append(rowshard(g[GOFF:GOFF + D]))
                leaves.append(nm)
            else:
                arrs.append(shard(g[:, GOFF:GOFF + D]))
                leaves.append(nm)
        (handle,), tok = _split_start([(_view_block, arrs, [a.shape for a in arrs])], "grads_%s_start" % names[0],
                                      x_after)
        sent.append((names[0], leaves, handle))
        return _tie(x_after, tok, "tie_" + names[0])

    r = _local_step(z, loss_target[0], modc, modx, norms, onw, hg_lb_full, wgk_k, bgk_k,
                    get_w_in, get_mix, get_ffn, send)
    grad_x = r["grad_x"][None]

    sm_pre, sm_mid, sm_fin = r["sm_pre"], r["sm_mid"], r["sm_final"]
    dmodc = jnp.stack([sm_pre[0], sm_pre[2], sm_mid[4], sm_mid[0], sm_mid[2], sm_fin[0]]).reshape(-1)
    dmodx = jnp.stack([sm_pre[1], sm_pre[3], sm_mid[5], sm_mid[1], sm_mid[3], sm_fin[1]]).reshape(-1)
    on = r["sm_post"][0].reshape(NH, DH)
    pieces = [dmodc, dmodx, sm_pre[4], sm_mid[7], sm_mid[6], sm_fin[2], on[:NH // 2].sum(0), on[NH // 2:].sum(0),
              r["d_lb"][:2].reshape(-1), _unlayout_wgk(r["d_wgk"]).reshape(-1), r["d_bgk"][0]]
    loss_local = (0.5 / D) * jnp.sum(r["loss_vec"])
    pieces.append(jnp.concatenate([loss_local.reshape(1), jnp.zeros((DH - 1,), F32)]))
    sizes = [p.shape[0] for p in pieces]
    pack = jnp.concatenate(pieces).reshape(-1, DH)
    moms = [(m_w_in, v_w_in), (m_w_br_hg, v_w_br_hg), (m_w_br_gla, v_w_br_gla), (m_w_out, v_w_out),
            (m_w_ff_gate, v_w_ff_gate), (m_w_ff_up, v_w_ff_up), (m_w_ff_down, v_w_ff_down)]
    names = ["w_in", "w_br_hg", "w_br_gla", "w_out", "w_ff_gate", "w_ff_up", "w_ff_down"]
    wmv = {nm: (w, m, v) for nm, w, (m, v) in zip(names, big, moms)}
    res = {}

    def update(nm):
        w, m, v = wmv[nm]
        if nm in ("w_ff_gate", "w_ff_up"):
            outs = _adamw(recv[nm], w, tr_(m), tr_(v), "adamw_" + nm)
            res[nm] = [jnp.swapaxes(o, 0, 1)[None] for o in outs]
        else:
            res[nm] = [o[None] for o in _adamw(recv[nm], w, m[0], v[0], "adamw_" + nm)]

    (small_handle, handle), tok = _split_start([whole([pack]), w_in_chunk(1)], "small_grads_start", pack)
    sent_w_in(1, handle)
    recv = {}
    for first, leaves, handle in sent:
        if not first.startswith("w_in"):
            recv.update(zip(leaves, _split_wait(handle, "grads_%s_wait" % first, tok)))
    update("w_ff_gate")
    update("w_ff_up")
    pack_all, = _split_wait(small_handle, "small_grads_wait", [res["w_ff_gate"][0], res["w_ff_up"][0]])
    tot = _sum_devices(pack_all).reshape(-1)
    offs = [sum(sizes[:i]) for i in range(len(sizes))]
    part = lambda i: tot[offs[i]:offs[i] + sizes[i]]
    dmodc_t, dmodx_t = part(0), part(1)
    g_b_mod = (dmodc_t + dmodx_t)[None]
    g_norms = [part(i)[None] for i in (2, 3, 4, 5)]
    g_hg_on, g_gla_on = part(6)[None], part(7)[None]
    lb0 = lax.dynamic_slice(part(8).reshape(2, HW), (0, me * (HW // N_DEV)), (2, HW // N_DEV))
    g_hg_lb = jnp.stack([lb0, -lb0])
    g_wgk = lax.dynamic_slice(part(9).reshape(2, 16, HW), (0, 0, me * (HW // N_DEV)), (2, 16, HW // N_DEV))[None]
    g_bgk = lax.dynamic_slice(part(10).reshape(2, HW), (0, me * (HW // N_DEV)), (2, HW // N_DEV))[None]
    loss = part(11)[0]

    dmx_all = pack_all.reshape(N_DEV, -1)[:, sizes[0]:sizes[0] + sizes[1]]
    d9 = jnp.concatenate([lax.dynamic_slice(dmodc_t[None], (0, me * n_mod), (1, n_mod)),
                          lax.dynamic_slice(dmx_all, (0, me * n_mod), (N_DEV, n_mod)),
                          jnp.zeros((16 - 1 - N_DEV, n_mod), F32)], axis=0)
    g_w_mod, dcc_part = _mod_bwd(a9, d9, w_mod[0])
    (cctx_handle, handle), tok = _split_start([whole([dcc_part]), w_in_chunk(2)], "c_ctx_start", dcc_part)
    sent_w_in(2, handle)
    recv["w_ff_down"] = _tie(recv["w_ff_down"], tok, "tie_down")
    update("w_ff_down")
    res["w_mod"] = [o[None] for o in _adamw(g_w_mod[None], w_mod[0], m_w_mod[0], v_w_mod[0], "adamw_w_mod")]
    for nm in ("w_out", "w_br_hg", "w_br_gla"):
        update(nm)
    dcc_all, = _split_wait(cctx_handle, "c_ctx_wait", [res["w_ff_down"][0], res["w_mod"][0]])
    g_c_ctx = _sum_devices(dcc_all)[0]

    small = [("c_ctx", c_ctx, m_c_ctx, v_c_ctx, g_c_ctx), ("b_mod", b_mod, m_b_mod, v_b_mod, g_b_mod),
             ("norm_pre1", norm_pre1, m_norm_pre1, v_norm_pre1, g_norms[0]),
             ("norm_post1", norm_post1, m_norm_post1, v_norm_post1, g_norms[1]),
             ("norm_pre2", norm_pre2, m_norm_pre2, v_norm_pre2, g_norms[2]),
             ("norm_post2", norm_post2, m_norm_post2, v_norm_post2, g_norms[3]),
             ("hg_lb", hg_lb, m_hg_lb, v_hg_lb, g_hg_lb), ("hg_onorm", hg_onorm, m_hg_onorm, v_hg_onorm, g_hg_on),
             ("gla_w_gk", gla_w_gk, m_gla_w_gk, v_gla_w_gk, g_wgk), ("gla_b_gk", gla_b_gk, m_gla_b_gk, v_gla_b_gk, g_bgk),
             ("gla_onorm", gla_onorm, m_gla_onorm, v_gla_onorm, g_gla_on)]
    flat = lambda k: jnp.concatenate([s[k].reshape(-1) for s in small]).reshape(-1, DH)
    outs = _adamw(flat(4)[None], flat(1), flat(2), flat(3), "adamw_small")
    off = 0
    for nm, w, _, _, _ in small:
        res[nm] = [o.reshape(-1)[off:off + w.size].reshape(w.shape) for o in outs]
        off += w.size

    done = [res[nm][0] for nm in names[1:]] + [res["w_mod"][0], outs[0]]
    sums = []
    for i, (first, leaves, handle) in enumerate(s for s in sent if s[0].startswith("w_in")):
        half = W_IN_GRAD_CHUNKS[i][0]
        land, = _split_wait(handle, "grads_%s_wait" % first, done, srcs=w_in_grad[half])
        w_in_grad[half] = handle["srcs"]
        sums.append(_sum_windows(land, "sum_windows%d" % i))
    major = lambda a: jnp.transpose(a, (2, 0, 1))
    outs = _adamw_rows3(jnp.concatenate(sums, axis=1), major(w_in), major(m_w_in), major(v_w_in), "adamw_w_in")
    res["w_in"] = [jnp.transpose(o, (1, 2, 0)) for o in outs]

    order = ["c_ctx", "w_mod", "b_mod", "norm_pre1", "norm_post1", "norm_pre2", "norm_post2", "w_in", "hg_lb",
             "hg_onorm", "gla_w_gk", "gla_b_gk", "gla_onorm", "w_br_hg", "w_br_gla", "w_out", "w_ff_gate", "w_ff_up",
             "w_ff_down"]
    return (loss, grad_x, *[res[n][k] for k in range(4) for n in order])
```

```python
import functools

import jax
import jax.numpy as jnp
from jax import lax
from jax.experimental import pallas as pl
from jax.experimental.pallas import tpu as pltpu

F32 = jnp.float32
BF16 = jnp.bfloat16
HI = lax.Precision.HIGHEST

N_DEV = 8
D = 1024
CTX = 256
HW = 512
DH = 128
NH = 8
D_FF = 2816
EPS = 1e-6
GLA_NORM = 16.0
CHUNK = 64
TR = 256
NCT = CTX // TR
W_IN_COLS = 7168
MAIN0 = 0
LR0 = 4608
GW = 1152
GOFF = 32
GATE_HG0 = LR0
GATE_GLA0 = LR0 + D
LEVELS = (32, 16, 8)
EXP_CLAMP = 80.0
VMEM_LIMIT = 48 * 1024 * 1024

ADAM_LR, ADAM_B1, ADAM_B2, ADAM_EPS, ADAM_WD, ADAM_STEP = 0.001, 0.9, 0.999, 1e-08, 0.01, 10


def _cp(*sem):
    return pltpu.CompilerParams(dimension_semantics=sem, vmem_limit_bytes=VMEM_LIMIT)


def _sig(x):
    return jax.nn.sigmoid(x)


def _silu(x):
    return x * _sig(x)


def _dsilu(x):
    s = _sig(x)
    return s * (1.0 + x * (1.0 - s))


def _rstd(x):
    return lax.rsqrt(jnp.mean(x * x, axis=-1, keepdims=True) + EPS)


def _rms_bwd(a, y, r):
    return r * (a - y * (r * r) * jnp.mean(a * y, axis=-1, keepdims=True))


def _colsum(x):
    return jnp.sum(x, axis=0, keepdims=True)


def _dot(a, b, dims, precision=None):
    return lax.dot_general(a, b, (dims, ((), ())), preferred_element_type=F32, precision=precision)


NN = ((1,), (0,))
NT = ((1,), (1,))
TN = ((0,), (0,))

SCAN_HEADS_FWD = 4
SCAN_HEADS_BWD = 4


def _split_dot(m, x):
    mb = m.astype(BF16)
    x1 = x.astype(BF16)
    r1 = x - x1.astype(F32)
    x2 = r1.astype(BF16)
    x3 = (r1 - x2.astype(F32)).astype(BF16)
    return _dot(mb, x1, NN) + _dot(mb, x2, NN) + _dot(mb, x3, NN)


def _matmul(a, b, dims, out_dtype, name, tm, tn, tk, a_off=0, m_out=None):
    a_pair = isinstance(a, (tuple, list))
    as_ = list(a) if a_pair else [a]
    a = as_[0]
    pair = isinstance(b, (tuple, list))
    bs = list(b) if pair else [b]
    b1 = bs[0]
    rows = b1.shape[0] * len(bs)
    half = None
    if dims == NN:
        m, k, n = a.shape[0], rows, b1.shape[1]
        a_spec = pl.BlockSpec((tm, tk), lambda i, j, kk: (i, kk + a_off))
        half = b1.shape[0] // tk
        if a_pair:
            assert pair and a.shape[1] == b1.shape[0] and a_off == 0
            a_spec = [pl.BlockSpec((tm, tk), lambda i, j, kk: (i, jnp.minimum(kk, half - 1))),
                      pl.BlockSpec((tm, tk), lambda i, j, kk: (i, jnp.maximum(kk - half, 0)))]
        b_maps = [lambda i, j, kk: (kk, j)] if not pair else [
            lambda i, j, kk: (jnp.minimum(kk, half - 1), j), lambda i, j, kk: (jnp.maximum(kk - half, 0), j)]
        b_specs = [pl.BlockSpec((tk, tn), f) for f in b_maps]
        axis = 2
    elif dims == NT:
        m, k, n = a.shape[0], b1.shape[1], rows
        a_spec = pl.BlockSpec((tm, tk), lambda i, j, kk: (i, kk + a_off))
        half = b1.shape[0] // tn
        b_maps = [lambda i, j, kk: (j, kk)] if not pair else [
            lambda i, j, kk: (jnp.minimum(j, half - 1), kk), lambda i, j, kk: (jnp.maximum(j - half, 0), kk)]
        b_specs = [pl.BlockSpec((tn, tk), f) for f in b_maps]
        axis = 1
    else:
        assert not pair
        m, k = (a.shape[1] if m_out is None else m_out), a.shape[0]
        n = b1.shape[1]
        a_spec = pl.BlockSpec((tk, tm), lambda i, j, kk: (kk, i + a_off))
        b_specs = [pl.BlockSpec((tk, tn), lambda i, j, kk: (kk, j))]
    assert m % tm == 0 and n % tn == 0 and k % tk == 0, (name, m, n, k, tm, tn, tk)
    nk = k // tk
    nb = len(bs)
    na = len(as_)
    assert na == 1 or dims == NN

    def body(*refs):
        a_refs, refs = refs[:na], refs[na:]
        o_ref = refs[nb]
        if pair:
            bv = jnp.where(pl.program_id(axis) < half, refs[0][...], refs[1][...])
        else:
            bv = refs[0][...]
        av = a_refs[0][...] if na == 1 else jnp.where(pl.program_id(2) < half, a_refs[0][...], a_refs[1][...])
        part = _dot(av, bv, dims)
        if nk == 1:
            o_ref[...] = part.astype(o_ref.dtype)
            return
        acc_ref = refs[nb + 1]
        kk = pl.program_id(2)

        @pl.when(kk == 0)
        def _():
            acc_ref[...] = part

        @pl.when(kk > 0)
        def _():
            acc_ref[...] += part

        @pl.when(kk == nk - 1)
        def _():
            o_ref[...] = acc_ref[...].astype(o_ref.dtype)

    return pl.pallas_call(
        body,
        name=name,
        grid=(m // tm, n // tn, nk),
        in_specs=(a_spec if a_pair else [a_spec]) + b_specs,
        out_specs=pl.BlockSpec((tm, tn), lambda i, j, kk: (i, j)),
        out_shape=jax.ShapeDtypeStruct((m, n), out_dtype),
        scratch_shapes=[] if nk == 1 else [pltpu.VMEM((tm, tn), F32)],
        compiler_params=_cp("parallel", "parallel", "arbitrary"),
    )(*as_, *bs)


def _mm_gu_act(h, w_gate_t, w_up_t, name, tm):
    t = h.shape[0]
    tn = D_FF // 2

    def body(a_ref, bg_ref, bu_ref, u_ref, v_ref, act_ref):
        a = a_ref[...]
        u = _dot(a, bg_ref[...], NT)
        v = _dot(a, bu_ref[...], NT)
        u_ref[...] = u.astype(BF16)
        v_ref[...] = v.astype(BF16)
        act_ref[...] = (_silu(u) * v).astype(BF16)

    wspec = pl.BlockSpec((tn, D), lambda i, j: (j, 0))
    ospec = pl.BlockSpec((tm, tn), lambda i, j: (i, j))
    out = jax.ShapeDtypeStruct((t, D_FF), BF16)
    return pl.pallas_call(
        body, name=name, grid=(t // tm, D_FF // tn),
        in_specs=[pl.BlockSpec((tm, D), lambda i, j: (i, 0)), wspec, wspec],
        out_specs=[ospec] * 3, out_shape=[out] * 3,
        compiler_params=_cp("parallel", "parallel"),
    )(h, w_gate_t, w_up_t)


def _mm_down_dx_act(dy, w_down, u, v, name, tm):
    t = dy.shape[0]
    tn = D_FF // 2

    def body(a_ref, b_ref, u_ref, v_ref, du_ref, dv_ref):
        dact = _dot(a_ref[...], b_ref[...], NT)
        u = u_ref[...].astype(F32)
        du_ref[...] = (dact * v_ref[...].astype(F32) * _dsilu(u)).astype(BF16)
        dv_ref[...] = (dact * _silu(u)).astype(BF16)

    ospec = pl.BlockSpec((tm, tn), lambda i, j: (i, j))
    out = jax.ShapeDtypeStruct((t, D_FF), BF16)
    return pl.pallas_call(
        body, name=name, grid=(t // tm, D_FF // tn),
        in_specs=[pl.BlockSpec((tm, D), lambda i, j: (i, 0)), pl.BlockSpec((tn, D), lambda i, j: (j, 0)), ospec, ospec],
        out_specs=[ospec] * 2, out_shape=[out] * 2,
        compiler_params=_cp("parallel", "parallel"),
    )(dy, w_down, u, v)


def _row(c):
    return pl.BlockSpec((TR, c), lambda i: (i, 0))


def _rowcol(width, cb):
    return pl.BlockSpec((TR, width), lambda i: (i, cb))


def _full(shape):
    return pl.BlockSpec(shape, lambda i: (0,) * len(shape))


def _mod_row(mc_ref, mx_ref, k, is_ctx):
    return jnp.where(is_ctx, mc_ref[k:k + 1, :], mx_ref[k:k + 1, :])


def _z_specs():
    return [pl.BlockSpec((TR, D), lambda i: (jnp.minimum(i, NCT - 1), 0)),
            pl.BlockSpec((TR, D), lambda i: (jnp.maximum(i - NCT, 0), 0))]


def _z_tile(c_ref, x_ref, is_ctx):
    return jnp.where(is_ctx, c_ref[...], x_ref[...])


def _acc_row(ref, k, val):
    ref[k:k + 1, :] += val


def _acc_mod(ref, k, is_ctx, val):
    zero = jnp.zeros_like(val)
    ref[k:k + 1, :] += jnp.where(is_ctx, val, zero)
    ref[k + 1:k + 2, :] += jnp.where(is_ctx, zero, val)


def _prenorm(z, nw, modc, modx, i_shift, i_scale, name):
    t = z[0].shape[0] + z[1].shape[0]

    def body(zc_ref, zx_ref, nw_ref, mc_ref, mx_ref, h_ref):
        is_ctx = pl.program_id(0) < NCT
        x = _z_tile(zc_ref, zx_ref, is_ctx)
        n = x * _rstd(x) * nw_ref[...]
        h = n * (1.0 + _mod_row(mc_ref, mx_ref, i_scale, is_ctx)) + _mod_row(mc_ref, mx_ref, i_shift, is_ctx)
        h_ref[...] = h.astype(BF16)

    return pl.pallas_call(
        body, name=name, grid=(t // TR,),
        in_specs=_z_specs() + [_full((1, D)), _full((8, D)), _full((8, D))],
        out_specs=_row(D),
        out_shape=jax.ShapeDtypeStruct((t, D), BF16),
        compiler_params=_cp("parallel"),
    )(*z, nw, modc, modx)


def _hg_lb(lb_ref, d):
    a0 = lb_ref[0, d:d + 1, :]
    a1 = lb_ref[1, d:d + 1, :]
    mx = jnp.maximum(a0, a1)
    e0 = jnp.exp(a0 - mx)
    e1 = jnp.exp(a1 - mx)
    return e0 / (e0 + e1)


def _log_sigmoid(x):
    return jnp.minimum(x, 0.0) - jnp.log(1.0 + jnp.exp(-jnp.abs(x)))


def _gates_fwd(p, hg_lb, wgk, bgk):
    t = p.shape[0]
    seg = lambda j: _rowcol(HW, MAIN0 // HW + j)

    def body(hq_ref, hi_ref, hf_ref, hb_ref, gq_ref, gk_ref, gv_ref, lr_ref, lb_ref, wgk_ref, bgk_ref,
             q_ref, v_ref, kf_ref, kb_ref, gf_ref, gb_ref):
        q_ref[:, :HW] = _silu(hq_ref[...].astype(F32)).astype(BF16)
        q_ref[:, HW:] = (gq_ref[...].astype(F32) * (DH ** -0.5)).astype(BF16)
        v_ref[:, :HW] = hi_ref[...]
        v_ref[:, HW:] = gv_ref[...]
        xg = _dot(lr_ref[...].astype(BF16), wgk_ref[...], NN) + bgk_ref[...]
        for d, (raw_ref, k_ref, g_ref) in enumerate(((hf_ref, kf_ref, gf_ref), (hb_ref, kb_ref, gb_ref))):
            lbd = _hg_lb(lb_ref, d)
            f = lbd + (1.0 - lbd) * _sig(raw_ref[...].astype(F32))
            k_ref[:, :HW] = (1.0 - f).astype(BF16)
            k_ref[:, HW:] = gk_ref[...]
            g_ref[:, :HW] = jnp.log(f)
            g_ref[:, HW:] = _log_sigmoid(xg[:, d * HW:(d + 1) * HW]) * (1.0 / GLA_NORM)

    out = jax.ShapeDtypeStruct((t, D), F32)
    outb = jax.ShapeDtypeStruct((t, D), BF16)
    return pl.pallas_call(
        body, name="gates_fwd", grid=(t // TR,),
        in_specs=[seg(0), seg(1), seg(2), seg(3), seg(5), seg(6), seg(7), _rowcol(DH, LR0 // DH),
                  _full((2, 2, HW)), _full((DH, D)), _full((1, D))],
        out_specs=[_row(D)] * 6,
        out_shape=[outb] * 4 + [out] * 2,
        compiler_params=_cp("parallel"),
    )(p, p, p, p, p, p, p, p, hg_lb, wgk, bgk)


def _post_fwd(o_fw, o_bw, p, onw):
    t = o_fw.shape[0]

    def body(of_ref, ob_ref, g1_ref, g2_ref, w_ref, y_ref):
        for h in range(NH):
            sl = slice(h * DH, (h + 1) * DH)
            o = of_ref[:, sl] + ob_ref[:, sl]
            g_ref = g1_ref if h < NH // 2 else g2_ref
            gs = slice((h % (NH // 2)) * DH, (h % (NH // 2) + 1) * DH)
            n = o * _rstd(o) * w_ref[:, sl]
            y_ref[:, sl] = (n * _silu(g_ref[:, gs].astype(F32))).astype(BF16)

    return pl.pallas_call(
        body, name="post_fwd", grid=(t // TR,),
        in_specs=[_row(D), _row(D), _rowcol(HW, MAIN0 // HW + 4), _rowcol(HW, MAIN0 // HW + 8), _full((1, D))],
        out_specs=_row(D),
        out_shape=jax.ShapeDtypeStruct((t, D), BF16),
        compiler_params=_cp("parallel"),
    )(o_fw, o_bw, p, p, onw)


def _gate_window_specs(col0):
    return [_rowcol(HW, col0 // HW), _rowcol(HW, col0 // HW + 1), _rowcol(DH, (col0 + 2 * HW) // DH)]


def _gate_window(refs):
    return jnp.concatenate([r[...].astype(F32) for r in refs], axis=1)


def _branch_merge(y, w_hg, w_gla, p):
    t = y.shape[0]

    def body(y_ref, wh_ref, wg_ref, a0, a1, a2, b0, b1, b2, u1_ref, u2_ref, m_ref):
        u1 = _dot(y_ref[:, :HW], wh_ref[...], NN)
        u2 = _dot(y_ref[:, HW:], wg_ref[...], NN)
        u1_ref[...] = u1.astype(BF16)
        u2_ref[...] = u2.astype(BF16)
        m_ref[...] = (_sig(_gate_window((a0, a1, a2))) * u1 + _sig(_gate_window((b0, b1, b2))) * u2).astype(BF16)

    out = jax.ShapeDtypeStruct((t, GW), BF16)
    return pl.pallas_call(
        body, name="branch_merge", grid=(t // TR,),
        in_specs=[_row(D), _full((HW, GW)), _full((HW, GW))] + _gate_window_specs(GATE_HG0)
        + _gate_window_specs(GATE_GLA0),
        out_specs=[_row(GW)] * 3, out_shape=[out] * 3,
        compiler_params=_cp("parallel"),
    )(y, w_hg, w_gla, p, p, p, p, p, p)


def _mid_fwd(z, y1, nw_post, nw_pre, modc, modx):
    t = y1.shape[0]

    def body(zc_ref, zx_ref, y_ref, wpo_ref, wpr_ref, mc_ref, mx_ref, z1_ref, h_ref):
        is_ctx = pl.program_id(0) < NCT
        y = y_ref[...].astype(F32)
        z1 = _z_tile(zc_ref, zx_ref, is_ctx) + _mod_row(mc_ref, mx_ref, 2, is_ctx) * (y * _rstd(y) * wpo_ref[...])
        z1_ref[...] = z1
        n = z1 * _rstd(z1) * wpr_ref[...]
        h = n * (1.0 + _mod_row(mc_ref, mx_ref, 4, is_ctx)) + _mod_row(mc_ref, mx_ref, 3, is_ctx)
        h_ref[...] = h.astype(BF16)

    return pl.pallas_call(
        body, name="mid_fwd", grid=(t // TR,),
        in_specs=_z_specs() + [_row(D), _full((1, D)), _full((1, D)), _full((8, D)), _full((8, D))],
        out_specs=[_row(D), _row(D)],
        out_shape=[jax.ShapeDtypeStruct((t, D), F32), jax.ShapeDtypeStruct((t, D), BF16)],
        compiler_params=_cp("parallel"),
    )(*z, y1, nw_post, nw_pre, modc, modx)


def _final(z1, y2, target, nw, modc, modx):
    t = z1.shape[0]

    def body(z1_ref, y_ref, tg_ref, w_ref, mc_ref, mx_ref, dz_ref, dy_ref, loss_ref, sm_ref):
        i = pl.program_id(0)
        is_ctx = i < NCT

        @pl.when(i == 0)
        def _():
            loss_ref[...] = jnp.zeros_like(loss_ref)
            sm_ref[...] = jnp.zeros_like(sm_ref)

        g = _mod_row(mc_ref, mx_ref, 5, is_ctx)
        y = y_ref[...].astype(F32)
        r = _rstd(y)
        w = w_ref[...]
        yr = y * r
        n = yr * w
        e = z1_ref[...] + g * n - tg_ref[...]
        lat = jnp.where(is_ctx, 0.0, 1.0)
        loss_ref[...] += lat * _colsum(e * e)
        dz = e * (lat / D)
        dz_ref[...] = dz
        _acc_mod(sm_ref, 0, is_ctx, _colsum(dz * n))
        dn = dz * g
        _acc_row(sm_ref, 2, _colsum(dn * yr))
        dy_ref[...] = _rms_bwd(dn * w, y, r).astype(BF16)

    return pl.pallas_call(
        body, name="final", grid=(t // TR,),
        in_specs=[_row(D), _row(D), pl.BlockSpec((TR, D), lambda i: (jnp.maximum(i - NCT, 0), 0)),
                  _full((1, D)), _full((8, D)), _full((8, D))],
        out_specs=[_row(D), _row(D), _full((1, D)), _full((8, D))],
        out_shape=[jax.ShapeDtypeStruct((t, D), F32), jax.ShapeDtypeStruct((t, D), BF16),
                   jax.ShapeDtypeStruct((1, D), F32), jax.ShapeDtypeStruct((8, D), F32)],
        compiler_params=_cp("arbitrary"),
    )(z1, y2, target, nw, modc, modx)


def _mid_bwd(dh2, dz, z1, y1, nw_post, nw_pre, modc, modx):
    t = z1.shape[0]

    def body(dh_ref, dz_ref, z1_ref, y_ref, wpo_ref, wpr_ref, mc_ref, mx_ref, dzo_ref, dy_ref, sm_ref):
        i = pl.program_id(0)
        is_ctx = i < NCT

        @pl.when(i == 0)
        def _():
            sm_ref[...] = jnp.zeros_like(sm_ref)

        dh = dh_ref[...].astype(F32)
        z1 = z1_ref[...]
        r = _rstd(z1)
        zr = z1 * r
        wpr = wpr_ref[...]
        n = zr * wpr
        _acc_mod(sm_ref, 0, is_ctx, _colsum(dh))
        _acc_mod(sm_ref, 2, is_ctx, _colsum(dh * n))
        dn = dh * (1.0 + _mod_row(mc_ref, mx_ref, 4, is_ctx))
        _acc_row(sm_ref, 6, _colsum(dn * zr))
        dz1 = dz_ref[...] + _rms_bwd(dn * wpr, z1, r)
        dzo_ref[...] = dz1
        y = y_ref[...].astype(F32)
        r1 = _rstd(y)
        yr = y * r1
        wpo = wpo_ref[...]
        g = _mod_row(mc_ref, mx_ref, 2, is_ctx)
        _acc_mod(sm_ref, 4, is_ctx, _colsum(dz1 * (yr * wpo)))
        dn1 = dz1 * g
        _acc_row(sm_ref, 7, _colsum(dn1 * yr))
        dy_ref[...] = _rms_bwd(dn1 * wpo, y, r1).astype(BF16)

    return pl.pallas_call(
        body, name="mid_bwd", grid=(t // TR,),
        in_specs=[_row(D)] * 4 + [_full((1, D)), _full((1, D)), _full((8, D)), _full((8, D))],
        out_specs=[_row(D), _row(D), _full((8, D))],
        out_shape=[jax.ShapeDtypeStruct((t, D), F32), jax.ShapeDtypeStruct((t, D), BF16),
                   jax.ShapeDtypeStruct((8, D), F32)],
        compiler_params=_cp("arbitrary"),
    )(dh2, dz, z1, y1, nw_post, nw_pre, modc, modx)


def _pre_bwd(dh1, dz, z, nw, modc, modx):
    t = dh1.shape[0]

    def body(dh_ref, dz_ref, zc_ref, zx_ref, w_ref, mc_ref, mx_ref, dzo_ref, sm_ref):
        i = pl.program_id(0)
        is_ctx = i < NCT

        @pl.when(i == 0)
        def _():
            sm_ref[...] = jnp.zeros_like(sm_ref)

        dh = dh_ref[...].astype(F32)
        x = _z_tile(zc_ref, zx_ref, is_ctx)
        r = _rstd(x)
        xr = x * r
        w = w_ref[...]
        _acc_mod(sm_ref, 0, is_ctx, _colsum(dh))
        _acc_mod(sm_ref, 2, is_ctx, _colsum(dh * (xr * w)))
        dn = dh * (1.0 + _mod_row(mc_ref, mx_ref, 1, is_ctx))
        _acc_row(sm_ref, 4, _colsum(dn * xr))
        dzo_ref[...] = dz_ref[...] + _rms_bwd(dn * w, x, r)

    return pl.pallas_call(
        body, name="pre_bwd", grid=(t // TR,),
        in_specs=[_row(D)] * 2 + _z_specs() + [_full((1, D)), _full((8, D)), _full((8, D))],
        out_specs=[pl.BlockSpec((TR, D), lambda i: (jnp.maximum(i - NCT, 0), 0)), _full((8, D))],
        out_shape=[jax.ShapeDtypeStruct((t - CTX, D), F32), jax.ShapeDtypeStruct((8, D), F32)],
        compiler_params=_cp("arbitrary"),
    )(dh1, dz, *z, nw, modc, modx)


def _branch_merge_bwd(dm, p, u1, u2, w_hg, w_gla):
    t = dm.shape[0]

    def body(dm_ref, a0, a1, a2, b0, b1, b2, u1_ref, u2_ref, wh_ref, wg_ref, du1_ref, du2_ref, dg_ref, dyh_ref, dyg_ref):
        dm_ = dm_ref[...].astype(F32)
        s1 = _sig(_gate_window((a0, a1, a2)))
        s2 = _sig(_gate_window((b0, b1, b2)))
        du1 = (dm_ * s1).astype(BF16)
        du2 = (dm_ * s2).astype(BF16)
        du1_ref[...] = du1
        du2_ref[...] = du2
        dg_ref[:, :GW] = (dm_ * u1_ref[...].astype(F32) * s1 * (1.0 - s1)).astype(BF16)
        dg_ref[:, GW:] = (dm_ * u2_ref[...].astype(F32) * s2 * (1.0 - s2)).astype(BF16)
        dyh_ref[...] = _dot(du1, wh_ref[...], NT).astype(BF16)
        dyg_ref[...] = _dot(du2, wg_ref[...], NT).astype(BF16)

    return pl.pallas_call(
        body, name="branch_merge_bwd", grid=(t // TR,),
        in_specs=[_row(GW)] + _gate_window_specs(GATE_HG0) + _gate_window_specs(GATE_GLA0)
        + [_row(GW), _row(GW), _full((HW, GW)), _full((HW, GW))],
        out_specs=[_row(GW), _row(GW), _row(2 * GW), _row(HW), _row(HW)],
        out_shape=[jax.ShapeDtypeStruct((t, GW), BF16), jax.ShapeDtypeStruct((t, GW), BF16),
                   jax.ShapeDtypeStruct((t, 2 * GW), BF16), jax.ShapeDtypeStruct((t, HW), BF16),
                   jax.ShapeDtypeStruct((t, HW), BF16)],
        compiler_params=_cp("parallel"),
    )(dm, p, p, p, p, p, p, u1, u2, w_hg, w_gla)


def _post_bwd(dy_hg, dy_gla, o_fw, o_bw, p, onw):
    t = o_fw.shape[0]

    def body(d1_ref, d2_ref, of_ref, ob_ref, g1_ref, g2_ref, w_ref, do_ref, dg_ref, sm_ref):
        @pl.when(pl.program_id(0) == 0)
        def _():
            sm_ref[...] = jnp.zeros_like(sm_ref)

        for h in range(NH):
            sl = slice(h * DH, (h + 1) * DH)
            gs = slice((h % (NH // 2)) * DH, (h % (NH // 2) + 1) * DH)
            g_ref, d_ref = (g1_ref, d1_ref) if h < NH // 2 else (g2_ref, d2_ref)
            o = of_ref[:, sl] + ob_ref[:, sl]
            r = _rstd(o)
            orr = o * r
            w = w_ref[:, sl]
            gt = g_ref[:, gs].astype(F32)
            dy = d_ref[:, gs].astype(F32)
            dg_ref[:, sl] = (dy * (orr * w) * _dsilu(gt)).astype(BF16)
            dn = dy * _silu(gt)
            sm_ref[0:1, sl] += _colsum(dn * orr)
            do_ref[:, sl] = _rms_bwd(dn * w, o, r)

    return pl.pallas_call(
        body, name="post_bwd", grid=(t // TR,),
        in_specs=[_row(HW), _row(HW), _row(D), _row(D), _rowcol(HW, MAIN0 // HW + 4), _rowcol(HW, MAIN0 // HW + 8),
                  _full((1, D))],
        out_specs=[_row(D), _row(D), _full((8, D))],
        out_shape=[jax.ShapeDtypeStruct((t, D), F32), jax.ShapeDtypeStruct((t, D), BF16),
                   jax.ShapeDtypeStruct((8, D), F32)],
        compiler_params=_cp("arbitrary"),
    )(dy_hg, dy_gla, o_fw, o_bw, p, p, onw)


def _gates_bwd(p, hg_lb, wgk, bgk, dgm, dgo, dq_f, dq_b, dv_f, dv_b, dk_f, dk_b, dg_f, dg_b):
    t = p.shape[0]
    seg = lambda j: _rowcol(HW, MAIN0 // HW + j)

    def body(hq_ref, hf_ref, hb_ref, lr_ref, lb_ref, wgk_ref, bgk_ref, dgm_ref, dgo_ref,
             dqf_ref, dqb_ref, dvf_ref, dvb_ref, dkf_ref, dkb_ref, dgf_ref, dgb_ref,
             dp_ref, dlb_ref, dw_ref, db_ref):
        @pl.when(pl.program_id(0) == 0)
        def _():
            dlb_ref[...] = jnp.zeros_like(dlb_ref)
            dw_ref[...] = jnp.zeros_like(dw_ref)
            db_ref[...] = jnp.zeros_like(db_ref)

        c0 = MAIN0

        def put(j, val):
            dp_ref[:, c0 + j * HW:c0 + (j + 1) * HW] = val.astype(BF16)

        dq = dqf_ref[...].astype(F32) + dqb_ref[...].astype(F32)
        dv = dvf_ref[...].astype(F32) + dvb_ref[...].astype(F32)
        put(0, dq[:, :HW] * _dsilu(hq_ref[...].astype(F32)))
        put(1, dv[:, :HW])
        put(5, dq[:, HW:] * (DH ** -0.5))
        put(7, dv[:, HW:])
        put(6, dkf_ref[:, HW:].astype(F32) + dkb_ref[:, HW:].astype(F32))
        dp_ref[:, c0 + 4 * HW:c0 + 5 * HW] = dgo_ref[:, :HW]
        dp_ref[:, c0 + 8 * HW:c0 + 9 * HW] = dgo_ref[:, HW:]
        lr = lr_ref[...].astype(BF16)
        xg = _dot(lr, wgk_ref[...], NN) + bgk_ref[...]
        dxg = []
        for d, (raw_ref, dk_ref, dg_ref) in enumerate(((hf_ref, dkf_ref, dgf_ref), (hb_ref, dkb_ref, dgb_ref))):
            lbd = _hg_lb(lb_ref, d)
            s = _sig(raw_ref[...].astype(F32))
            f = lbd + (1.0 - lbd) * s
            df = dg_ref[:, :HW] / f - dk_ref[:, :HW].astype(F32)
            put(2 + d, df * (1.0 - lbd) * s * (1.0 - s))
            dlb_ref[d:d + 1, :] += _colsum(df * (1.0 - s)) * (lbd * (1.0 - lbd))
            dxg.append(dg_ref[:, HW:] * (1.0 / GLA_NORM) * _sig(-xg[:, d * HW:(d + 1) * HW]))
        dxg = jnp.concatenate(dxg, axis=1)
        db_ref[0:1, :] += _colsum(dxg)
        dxg_b = dxg.astype(BF16)
        dw_ref[...] += _dot(lr, dxg_b, TN)
        dlr = _dot(dxg_b, wgk_ref[...], NT)
        dp_ref[:, LR0:LR0 + DH] = (dlr + dgm_ref[:, :DH].astype(F32)).astype(BF16)
        dp_ref[:, LR0 + DH:GATE_GLA0] = dgm_ref[:, DH:D]
        dp_ref[:, GATE_GLA0:GATE_GLA0 + DH] = dgm_ref[:, D:GW] + dgm_ref[:, GW:GW + DH]
        dp_ref[:, GATE_GLA0 + DH:GATE_GLA0 + GW] = dgm_ref[:, GW + DH:]
        dp_ref[:, GATE_GLA0 + GW:] = jnp.zeros((TR, W_IN_COLS - GATE_GLA0 - GW), BF16)

    return pl.pallas_call(
        body, name="gates_bwd", grid=(t // TR,),
        in_specs=[seg(0), seg(2), seg(3), _rowcol(DH, LR0 // DH), _full((2, 2, HW)), _full((DH, D)), _full((1, D)),
                  _row(2 * GW), _row(D)] + [_row(D)] * 8,
        out_specs=[_row(W_IN_COLS), _full((8, HW)), _full((DH, D)), _full((8, D))],
        out_shape=[jax.ShapeDtypeStruct((t, W_IN_COLS), BF16), jax.ShapeDtypeStruct((8, HW), F32),
                   jax.ShapeDtypeStruct((DH, D), F32), jax.ShapeDtypeStruct((8, D), F32)],
        compiler_params=_cp("arbitrary"),
    )(p, p, p, p, hg_lb, wgk, bgk, dgm, dgo, dq_f, dq_b, dv_f, dv_b, dk_f, dk_b, dg_f, dg_b)


def _scan_consts(rev):
    r = lax.broadcasted_iota(jnp.int32, (CHUNK, CHUNK), 0)
    u = lax.broadcasted_iota(jnp.int32, (CHUNK, CHUNK), 1)
    rp = lax.broadcasted_iota(jnp.int32, (CHUNK, 1), 0)
    if rev:
        r, u, rp = CHUNK - 1 - r, CHUNK - 1 - u, CHUNK - 1 - rp
    tri = jnp.where(u <= r, 1.0, 0.0).astype(F32)
    tri_t = jnp.where(r <= u, 1.0, 0.0).astype(F32)
    lv = []
    for b in LEVELS:
        sh = b.bit_length() - 1
        pair = ((r >> sh) == (u >> sh) + 1) & (((u >> sh) & 1) == 0)
        pair_t = ((u >> sh) == (r >> sh) + 1) & (((r >> sh) & 1) == 0)
        tside = ((rp >> sh) & 1) == 1
        lv.append((pair, pair_t, tside, jnp.where(tside, 1.0, -1.0).astype(F32)))
    bd = LEVELS[-1].bit_length() - 1
    diag = ((r >> bd) == (u >> bd)) & (u <= r)
    diag_t = ((r >> bd) == (u >> bd)) & (r <= u)
    return tri, tri_t, lv, diag, diag_t


def _row_of(pos, rev):
    return CHUNK - 1 - pos if rev else pos


def _chunk_terms(cum, b_scr, consts, rev):
    _, _, lv, _, _ = consts
    terms = []
    for b, (_, _, _, sgn) in zip(LEVELS, lv):
        pieces = []
        for j in range(CHUNK // (2 * b)):
            row = _row_of(2 * b * j + b - 1, rev)
            pieces.append(jnp.broadcast_to(b_scr[row:row + 1, :], (2 * b, DH)))
        if rev:
            pieces = pieces[::-1]
        bnd = pieces[0] if len(pieces) == 1 else jnp.concatenate(pieces, axis=0)
        terms.append(jnp.exp((cum - bnd) * sgn))
    b = LEVELS[-1]
    pieces = []
    for j in range(CHUNK // b):
        if j == 0:
            pieces.append(jnp.zeros((b, DH), F32))
        else:
            row = _row_of(b * j - 1, rev)
            pieces.append(jnp.broadcast_to(b_scr[row:row + 1, :], (b, DH)))
    if rev:
        pieces = pieces[::-1]
    start = jnp.concatenate(pieces, axis=0)
    wq = jnp.exp(jnp.minimum(cum - start, 0.0))
    wk = jnp.exp(jnp.minimum(start - cum, EXP_CLAMP))
    terms.append((wq, wk))
    return terms


def _run_staged(units):
    live = list(units)
    while live:
        nxt = []
        for u in live:
            try:
                next(u)
                nxt.append(u)
            except StopIteration:
                pass
        live = nxt


SCAN_TB = 256
SCAN_CB = SCAN_TB // CHUNK


def _block_order(i, ntb, rev):
    nctx = CTX // SCAN_TB
    if not rev:
        return i
    return jnp.where(i < nctx, nctx - 1 - i, ntb - 1 - (i - nctx))


def _chunk_in_block(j, rev):
    return SCAN_CB - 1 - j if rev else j


def _scan_fwd(q, k, v, g, rev):
    t = q.shape[0]
    nc = t // CHUNK
    hpb = SCAN_HEADS_FWD

    def body(q_ref, k_ref, v_ref, g_ref, o_ref, st_ref, s_scr, b_scr):
        consts = _scan_consts(rev)
        _, _, lv, diag, _ = consts
        masks = [lvl[0] for lvl in lv] + [diag]

        @pl.when(pl.program_id(1) == 0)
        def _():
            s_scr[...] = jnp.zeros_like(s_scr)

        tri = consts[0]
        state = {hh: s_scr[hh] for hh in range(hpb)}

        def unit(hh, j):
            sl = slice(hh * DH, (hh + 1) * DH)
            c = _chunk_in_block(j, rev)
            rows = slice(c * CHUNK, (c + 1) * CHUNK)
            b_ref = b_scr.at[hh * SCAN_CB + j]
            qc, kc, vc, gc = q_ref[rows, sl], k_ref[rows, sl], v_ref[rows, sl], g_ref[rows, sl]
            cum = _split_dot(tri, gc)
            b_ref[...] = cum
            yield
            terms = _chunk_terms(cum, b_ref, consts, rev)
            qf, kf = qc.astype(F32), kc.astype(F32)
            xs = [(jnp.where(tside, qf, kf) * w).astype(BF16) for w, (_, _, tside, _) in zip(terms[:-1], lv)]
            qd, kd = (qf * terms[-1][0]).astype(BF16), (kf * terms[-1][1]).astype(BF16)
            tot = _colsum(gc)
            qe = (qf * jnp.exp(cum)).astype(BF16)
            ke = (kf * jnp.exp(tot - cum)).astype(BF16)
            vb = vc.astype(BF16)
            yield
            scs = [_dot(x, x, NT) for x in xs] + [_dot(qd, kd, NT)]
            kv = _dot(vb, ke, TN)
            yield
            a = jnp.zeros((CHUNK, CHUNK), F32)
            for sc, m in zip(scs, masks):
                a = a + jnp.where(m, sc, 0.0)
            o_intra = _dot(a.astype(BF16), vb, NN)
            yield
            st = state[hh]
            st_ref[hh, c] = st
            o_ref[rows, sl] = o_intra + _dot(qe, st.astype(BF16), NT)
            state[hh] = st * jnp.exp(tot) + kv
            yield

        _run_staged([unit(hh, j) for hh in range(hpb) for j in range(SCAN_CB)])
        for hh in range(hpb):
            s_scr[hh] = state[hh]

    ntb = t // SCAN_TB
    col = pl.BlockSpec((SCAN_TB, hpb * DH), lambda h, i: (_block_order(i, ntb, rev), h))
    return pl.pallas_call(
        body, name="scan_fwd_" + ("bw" if rev else "fw"), grid=(NH // hpb, ntb),
        in_specs=[col] * 4,
        out_specs=[col, pl.BlockSpec((hpb, SCAN_CB, DH, DH), lambda h, i: (h, _block_order(i, ntb, rev), 0, 0))],
        out_shape=[jax.ShapeDtypeStruct((t, D), F32), jax.ShapeDtypeStruct((NH, nc, DH, DH), F32)],
        scratch_shapes=[pltpu.VMEM((hpb, DH, DH), F32), pltpu.VMEM((hpb * SCAN_CB, CHUNK, DH), F32)],
        compiler_params=_cp("parallel", "arbitrary"),
    )(q, k, v, g)


def _scan_bwd(q, k, v, g, do, states, rev):
    t = q.shape[0]
    nc = t // CHUNK
    hpb = SCAN_HEADS_BWD

    def body(q_ref, k_ref, v_ref, g_ref, do_ref, st_ref, dq_ref, dk_ref, dv_ref, dg_ref, ds_scr, b_scr):
        consts = _scan_consts(rev)
        _, tri_t, lv, diag, diag_t = consts
        masks = [(lvl[0], lvl[1]) for lvl in lv] + [(diag, diag_t)]
        @pl.when(pl.program_id(1) == 0)
        def _():
            ds_scr[...] = jnp.zeros_like(ds_scr)

        tri = consts[0]
        dstate = {hh: ds_scr[hh] for hh in range(hpb)}

        def unit(hh, jj):
            sl = slice(hh * DH, (hh + 1) * DH)
            c = _chunk_in_block(SCAN_CB - 1 - jj, rev)
            rows = slice(c * CHUNK, (c + 1) * CHUNK)
            b_ref = b_scr.at[hh * SCAN_CB + jj]
            qc, kc, vc, gc = q_ref[rows, sl], k_ref[rows, sl], v_ref[rows, sl], g_ref[rows, sl]
            dob = do_ref[rows, sl].astype(BF16)
            vb = vc.astype(BF16)
            cum = _split_dot(tri, gc)
            b_ref[...] = cum
            da = _dot(dob, vb, NT)
            da_t = _dot(vb, dob, NT)
            yield
            terms = _chunk_terms(cum, b_ref, consts, rev)
            qf, kf = qc.astype(F32), kc.astype(F32)
            xs = [(jnp.where(tside, qf, kf) * w).astype(BF16) for w, (_, _, tside, _) in zip(terms[:-1], lv)]
            wqd, wkd = terms[-1]
            qdb, kdb = (qf * wqd).astype(BF16), (kf * wkd).astype(BF16)
            tot = _colsum(gc)
            e_tot = jnp.exp(tot)
            e_b = jnp.exp(cum)
            e_t = jnp.exp(tot - cum)
            qeb = (qf * e_b).astype(BF16)
            keb = (kf * e_t).astype(BF16)
            dsym = [(jnp.where(m, da, 0.0) + jnp.where(m_t, da_t, 0.0)).astype(BF16) for m, m_t in masks[:-1]]
            dad = (jnp.where(diag, da, 0.0).astype(BF16), jnp.where(diag_t, da_t, 0.0).astype(BF16))
            yield
            sym = [_dot(x, x, NT) for x in xs]
            dxs = [_dot(d, x, NN) for d, x in zip(dsym, xs)]
            at_d = _dot(kdb, qdb, NT)
            dqt_d = _dot(dad[0], kdb, NN)
            dkt_d = _dot(dad[1], qdb, NN)
            qd = _dot(dob, qeb, TN)
            yield
            a_t = jnp.where(diag_t, at_d, 0.0)
            dq = dqt_d * wqd
            dk = dkt_d * wkd
            db = dqt_d * qdb.astype(F32) - dkt_d * kdb.astype(F32)
            for s, dx, x, w, (_, m_t, tside, sgn) in zip(sym, dxs, xs, terms[:-1], lv):
                a_t = a_t + jnp.where(m_t, s, 0.0)
                dxw = dx * w
                dq = dq + jnp.where(tside, dxw, 0.0)
                dk = dk + jnp.where(tside, 0.0, dxw)
                db = db + (dx * x.astype(F32)) * sgn
            dv_intra = _dot(a_t.astype(BF16), dob, NN)
            st = st_ref[hh, c]
            stb = st.astype(BF16)
            dqe = _dot(dob, stb, NN)
            yield
            dst = dstate[hh]
            dstb = dst.astype(BF16)
            dstate[hh] = dst * e_tot + qd
            dv_ref[rows, sl] = (dv_intra + _dot(keb, dstb, NT)).astype(BF16)
            dke = _dot(vb, dstb, NN)
            yield
            qe = qeb.astype(F32)
            ke = keb.astype(F32)
            dq_ref[rows, sl] = (dq + dqe * e_b).astype(BF16)
            dk_ref[rows, sl] = (dk + dke * e_t).astype(BF16)
            db = db + dqe * qe - dke * ke
            dtot = _colsum(dstb.astype(F32) * stb.astype(F32)) * e_tot + _colsum(dke * ke)
            dg_ref[rows, sl] = _split_dot(tri_t, db) + dtot
            yield

        _run_staged([unit(hh, jj) for hh in range(hpb) for jj in range(SCAN_CB)])
        for hh in range(hpb):
            ds_scr[hh] = dstate[hh]

    ntb = t // SCAN_TB
    blk = lambda i: _block_order(ntb - 1 - i, ntb, rev)
    col = pl.BlockSpec((SCAN_TB, hpb * DH), lambda h, i: (blk(i), h))
    out = jax.ShapeDtypeStruct((t, D), F32)
    outb = jax.ShapeDtypeStruct((t, D), BF16)
    return pl.pallas_call(
        body, name="scan_bwd_" + ("bw" if rev else "fw"), grid=(NH // hpb, ntb),
        in_specs=[col] * 5 + [pl.BlockSpec((hpb, SCAN_CB, DH, DH), lambda h, i: (h, blk(i), 0, 0))],
        out_specs=[col] * 4,
        out_shape=[outb] * 3 + [out],
        scratch_shapes=[pltpu.VMEM((hpb, DH, DH), F32), pltpu.VMEM((hpb * SCAN_CB, CHUNK, DH), F32)],
        compiler_params=_cp("parallel", "arbitrary"),
    )(q, k, v, g, do, states)


W_IN_GRAD_CHUNKS = (("a", (0, 512)), ("b", (0, 256)), ("b", (256, 512)))
W_IN_REF = 6688
W_IN_PAD = 896
W_IN_PIECE = 256
W_IN_STAGES = (3, 4)


def _assemble_w_in(g, rows, prev, name):
    n, r, wp = g.shape
    tr = W_IN_PIECE
    tiles = wp // DH
    first = rows[0] // tr

    def body(g_ref, *refs):
        o_ref = refs[-1]
        lane = lax.broadcasted_iota(jnp.int32, (tr, DH), 1)
        for t in range(W_IN_COLS // DH):
            acc = None
            for j in range(n):
                c = DH * t - W_IN_SHARD * j
                if c <= -DH or c >= W_IN_SHARD:
                    continue
                k, s = divmod(c, DH)
                lo = g_ref[j, :, k * DH:(k + 1) * DH] if 0 <= k < tiles else None
                hi = g_ref[j, :, (k + 1) * DH:(k + 2) * DH] if s and 0 <= k + 1 < tiles else None
                if s:
                    zero = jnp.zeros((tr, DH), g.dtype)
                    lo = zero if lo is None else pltpu.roll(lo, DH - s, 1)
                    hi = zero if hi is None else pltpu.roll(hi, DH - s, 1)
                    part = jnp.where(lane < DH - s, lo, hi)
                else:
                    part = lo
                acc = part if acc is None else acc + part
            o_ref[:, t * DH:(t + 1) * DH] = jnp.zeros((tr, DH), g.dtype) if acc is None else acc

    held = [] if prev is None else [prev]
    return pl.pallas_call(
        body, name=name, grid=((rows[1] - rows[0]) // tr,),
        in_specs=[pl.BlockSpec((n, tr, wp), lambda i: (0, first + i, 0))] + [pl.BlockSpec(memory_space=pl.ANY)] * len(held),
        out_specs=pl.BlockSpec((tr, W_IN_COLS), lambda i: (first + i, 0)),
        out_shape=jax.ShapeDtypeStruct((r, W_IN_COLS), g.dtype),
        input_output_aliases={1: 0} if held else {},
        compiler_params=_cp("parallel"),
    )(g, *held)


def _gate_cols(w):
    return jnp.pad(w, ((0, 0), (GOFF, GW - GOFF - D)))


def _gate_rows(w):
    return jnp.pad(w, ((GOFF, GW - GOFF - D), (0, 0)))


def _layout_wgk(w):
    r = w.shape[1]
    top = jnp.concatenate([w[0], jnp.zeros_like(w[0])], axis=1)
    bot = jnp.concatenate([jnp.zeros_like(w[1]), w[1]], axis=1)
    return jnp.concatenate([top, bot, jnp.zeros((DH - 2 * r, D), w.dtype)], axis=0)


def _unlayout_wgk(d, r=16):
    return jnp.stack([d[:r, :HW], d[r:2 * r, HW:]])


def _local_step(z, target, modc, modx, norms, onw, hg_lb, wgk, bgk, get_w_in, get_mix, get_ffn, send):
    n_pre1, n_post1, n_pre2, n_post2 = norms
    t = z[0].shape[0] + z[1].shape[0]
    tm = 1152 if t % 1152 == 0 else 256
    h1 = _prenorm(z, n_pre1, modc, modx, 0, 1, "prenorm1")
    w_in = get_w_in(h1)
    p = _matmul(h1, w_in, NN, BF16, "mm_in", t, 1024, D)
    q, v, k_f, k_b, g_f, g_b = _gates_fwd(p, hg_lb, wgk, bgk)
    o_f, st_f = _scan_fwd(q, k_f, v, g_f, False)
    o_b, st_b = _scan_fwd(q, k_b, v, g_b, True)
    y = _post_fwd(o_f, o_b, p, onw)
    w_br_hg, w_br_gla, w_out = get_mix(y)
    u1, u2, merged = _branch_merge(y, w_br_hg, w_br_gla, p)
    y1 = _matmul(merged, w_out, NN, BF16, "mm_out", tm, 512, GW)
    z1, h2 = _mid_fwd(z, y1, n_post1, n_pre2, modc, modx)
    w_gu_t, w_down = get_ffn(h2)
    u, v_ff, act = _mm_gu_act(h2, w_gu_t[0], w_gu_t[1], "mm_gu", tm)
    y2 = _matmul(act, w_down, NN, BF16, "mm_down", t, 512, D_FF)
    dz, dy2, loss_vec, sm_final = _final(z1, y2, target, n_post2, modc, modx)
    du, dv_ff = _mm_down_dx_act(dy2, w_down, u, v_ff, "mm_down_dx", tm)
    d_w_down = _matmul(act, dy2, TN, BF16, "mm_down_dw", D_FF // 2, 1024, t)
    dh2 = _matmul((du, dv_ff), w_gu_t, NN, BF16, "mm_gu_dx", tm, 512, D_FF)
    d_w_gate_t = _matmul(du, h2, TN, BF16, "mm_gate_dw", D_FF // 2, 1024, t)
    d_w_up_t = _matmul(dv_ff, h2, TN, BF16, "mm_up_dw", D_FF // 2, 1024, t)
    dh2 = send(("w_down", "w_gate_t", "w_up_t"), (d_w_down, d_w_gate_t, d_w_up_t), dh2)
    dz, dy1, sm_mid = _mid_bwd(dh2, dz, z1, y1, n_post1, n_pre2, modc, modx)
    dmerged = _matmul(dy1, w_out, NT, BF16, "mm_out_dx", tm, GW, D)
    d_w_out = _matmul(merged, dy1, TN, BF16, "mm_out_dw", GW, 512, t)
    du1, du2, dgm, dy_hg, dy_gla = _branch_merge_bwd(dmerged, p, u1, u2, w_br_hg, w_br_gla)
    d_w_br_hg = _matmul(y, du1, TN, BF16, "mm_br_hg_dw", HW, GW, t, a_off=0, m_out=HW)
    d_w_br_gla = _matmul(y, du2, TN, BF16, "mm_br_gla_dw", HW, GW, t, a_off=1, m_out=HW)
    dy_hg = send(("w_out", "w_br_hg", "w_br_gla"), (d_w_out, d_w_br_hg, d_w_br_gla), dy_hg)
    do, dgo, sm_post = _post_bwd(dy_hg, dy_gla, o_f, o_b, p, onw)
    dq_f, dk_f, dv_f, dg_f = _scan_bwd(q, k_f, v, g_f, do, st_f, False)
    dq_b, dk_b, dv_b, dg_b = _scan_bwd(q, k_b, v, g_b, do, st_b, True)
    dp, d_lb, d_wgk, d_bgk = _gates_bwd(p, hg_lb, wgk, bgk, dgm, dgo, dq_f, dq_b, dv_f, dv_b, dk_f, dk_b, dg_f, dg_b)
    d_w_in_a = _matmul(h1, dp, TN, BF16, "mm_in_dw_a", 512, 1024, t, a_off=0, m_out=D // 2)
    dp = send(("w_in_a",), (d_w_in_a,), dp)
    d_w_in_b = _matmul(h1, dp, TN, BF16, "mm_in_dw_b", 512, 1024, t, a_off=1, m_out=D // 2)
    dp = send(("w_in_b",), (d_w_in_b,), dp)
    dh1 = _matmul(dp, w_in, NT, BF16, "mm_in_dx", tm, 512, W_IN_COLS // 2)
    grad_x, sm_pre = _pre_bwd(dh1, dz, z, n_pre1, modc, modx)
    return dict(loss_vec=loss_vec, grad_x=grad_x, sm_final=sm_final, sm_mid=sm_mid, sm_post=sm_post, sm_pre=sm_pre,
                d_lb=d_lb, d_wgk=d_wgk, d_bgk=d_bgk)


MESH = pl.DeviceIdType.MESH
ANY = pl.BlockSpec(memory_space=pl.ANY)
N_REL = N_DEV - 1


def _place():
    return lax.axis_index("x"), lax.axis_index("y"), lax.axis_index("c")


def _slot(p):
    return 4 * p[0] + 2 * p[1] + p[2]


HBM = pl.BlockSpec(memory_space=pltpu.HBM)
SEM = pl.BlockSpec(memory_space=pltpu.SEMAPHORE)
EFFECT = pltpu.SideEffectType.DATAFLOW_SIDE_EFFECTING


def _peer_of(x, y, c, k):
    flip = lambda v, bit: 1 - v if bit else v
    return flip(x, k & 4), flip(y, k & 2), flip(c, k & 1)


def _view_whole(src, slot):
    return src


def _view_near(src, slot):
    return src


_view_near.peers = (1, 2, 4, 6)


def _view_near_rows(rows):
    def view(src, slot):
        return src.at[pl.ds(rows[0], rows[1] - rows[0])]
    view.peers = _view_near.peers
    view.land = lambda land, slot: land.at[slot, pl.ds(rows[0], rows[1] - rows[0])]
    return view


def _view_block(src, slot):
    return src.at[slot]


W_IN_SHARD = W_IN_REF // N_DEV


def _view_window(rows):
    def view(src, slot):
        col0 = pl.multiple_of((W_IN_SHARD * slot // DH) * DH, DH)
        return src.at[pl.ds(rows[0], rows[1] - rows[0]), pl.ds(col0, D)]
    return view


def _split_copies(view, srcs, lands, send_sems, recv_sems, local_sems):
    x, y, c = _place()
    me = _slot((x, y, c))
    into = getattr(view, "land", lambda land, slot: land.at[slot])
    local, sends, waits = [], [], []
    for a, (src, land) in enumerate(zip(srcs, lands)):
        local.append(pltpu.make_async_copy(view(src, me), into(land, me), local_sems.at[a]))
        for k in getattr(view, "peers", range(1, N_DEV)):
            peer = _peer_of(x, y, c, k)
            mine = view(src, _slot(peer))
            sems = dict(send_sem=send_sems.at[N_REL * a + k - 1], recv_sem=recv_sems.at[N_REL * a + k - 1],
                        device_id=peer, device_id_type=MESH)
            sends.append(pltpu.make_async_remote_copy(src_ref=mine, dst_ref=into(land, me), **sems))
            waits.append(pltpu.make_async_remote_copy(src_ref=mine, dst_ref=into(land, _slot(peer)), **sems))
    return local, sends, waits


def _split_start(groups, name, after):
    built = []
    for view, srcs, lands in groups:
        lands = [lax.empty(l, s.dtype) if isinstance(l, tuple) else l for l, s in zip(lands, srcs)]
        built.append((view, list(srcs), lands))
    bufs = [b for _, srcs, lands in built for b in srcs + lands]
    nb, ng = len(bufs), len(built)

    def body(*refs):
        buf_refs, sem_refs, token = refs[:nb], refs[nb + 1:nb + 1 + 3 * ng], refs[-1]
        pos = 0
        for i, (view, srcs, _) in enumerate(built):
            n = len(srcs)
            local, sends, _ = _split_copies(view, buf_refs[pos:pos + n], buf_refs[pos + n:pos + 2 * n],
                                            *sem_refs[3 * i:3 * i + 3])
            pos += 2 * n
            for cp in local + sends:
                cp.start()
        token[...] = jnp.zeros_like(token)

    sems = []
    for _, srcs, _ in built:
        n = len(srcs)
        sems += [pltpu.SemaphoreType.DMA((N_REL * n,)), pltpu.SemaphoreType.DMA((N_REL * n,)),
                 pltpu.SemaphoreType.DMA((n,))]
    hbm = lambda a: pltpu.with_memory_space_constraint(a, pltpu.HBM)
    out = pl.pallas_call(
        body, name=name,
        out_shape=(*sems, *[pltpu.HBM(b.shape, b.dtype) for b in bufs], jax.ShapeDtypeStruct((8, DH), F32)),
        in_specs=[HBM] * nb + [ANY],
        out_specs=(*([SEM] * (3 * ng)), *([HBM] * nb), pl.BlockSpec(memory_space=pltpu.VMEM)),
        input_output_aliases={i: 3 * ng + i for i in range(nb)},
        compiler_params=pltpu.CompilerParams(has_side_effects=EFFECT),
    )(*[hbm(b) for b in bufs], after)
    handles, pos = [], 3 * ng
    for i, (view, srcs, _) in enumerate(built):
        n = len(srcs)
        handles.append(dict(view=view, n=n, sems=out[3 * i:3 * i + 3], srcs=list(out[pos:pos + n]),
                            lands=list(out[pos + n:pos + 2 * n])))
        pos += 2 * n
    return handles, out[-1]


def _split_wait(handle, name, after, srcs=None, lands=None):
    view, n, sems = handle["view"], handle["n"], handle["sems"]
    srcs = handle["srcs"] if srcs is None else srcs
    lands = handle["lands"] if lands is None else lands
    afters = list(after) if isinstance(after, (list, tuple)) else [after]

    def body(*refs):
        src_refs, land_refs = refs[:n], refs[n:2 * n]
        send_sems, recv_sems, local_sems = refs[2 * n:2 * n + 3]
        local, _, waits = _split_copies(view, src_refs, land_refs, send_sems, recv_sems, local_sems)
        for cp in waits:
            cp.wait_send()
            cp.wait_recv()
        for cp in local:
            cp.wait()

    out = pl.pallas_call(
        body, name=name,
        out_shape=(*[pltpu.HBM(s.shape, s.dtype) for s in srcs], *[pltpu.HBM(l.shape, l.dtype) for l in lands]),
        in_specs=[HBM] * (2 * n) + [SEM, SEM, SEM] + [ANY] * len(afters),
        out_specs=tuple([HBM] * (2 * n)),
        input_output_aliases={i: i for i in range(2 * n)},
        compiler_params=pltpu.CompilerParams(has_side_effects=EFFECT),
    )(*srcs, *lands, *sems, *afters)
    handle["srcs"] = list(out[:n])
    return list(out[n:])


def _tie(x, token, name):
    def body(x_ref, t_ref, o_ref):
        pass

    return pl.pallas_call(
        body, name=name, out_shape=jax.ShapeDtypeStruct(x.shape, x.dtype),
        in_specs=[ANY, ANY], out_specs=ANY, input_output_aliases={0: 0},
    )(x, token)


def _forward_to_sibling(land, name, rows):
    def body(land_ref, out_ref, send_sems, recv_sems):
        x, y, c = _place()
        sibling = (x, y, 1 - c)
        chips = [(1 - x, y), (x, 1 - y), (1 - x, 1 - y)]
        piece = pl.ds(rows[0], rows[1] - rows[0])

        def copy(j, core):
            blk = _slot((*chips[j], core))
            return pltpu.make_async_remote_copy(src_ref=land_ref.at[blk, piece], dst_ref=out_ref.at[blk, piece],
                                                send_sem=send_sems.at[j], recv_sem=recv_sems.at[j],
                                                device_id=sibling, device_id_type=MESH)

        sends = [copy(j, c) for j in range(3)]
        for cp in sends:
            cp.start()
        for j in range(3):
            copy(j, 1 - c).wait_recv()
        for cp in sends:
            cp.wait_send()

    return pl.pallas_call(
        body, name=name, in_specs=[ANY], out_specs=ANY, input_output_aliases={0: 0},
        out_shape=jax.ShapeDtypeStruct(land.shape, land.dtype),
        scratch_shapes=[pltpu.SemaphoreType.DMA((3,)), pltpu.SemaphoreType.DMA((3,))],
    )(land)


def _mod_fwd(a, w, b):
    def body(a_ref, w_ref, b_ref, o_ref):
        o_ref[...] = _dot(_silu(a_ref[...]), w_ref[...], NN, precision=HI) + b_ref[...]

    return pl.pallas_call(
        body, name="mod_fwd", out_shape=jax.ShapeDtypeStruct((a.shape[0], w.shape[1]), F32),
        compiler_params=pltpu.CompilerParams(vmem_limit_bytes=VMEM_LIMIT),
    )(a, w, b)


def _mod_bwd(a, d, w):
    def body(a_ref, d_ref, w_ref, dw_ref, dc_ref):
        av = a_ref[...]
        dv = d_ref[...]
        dw_ref[...] = _dot(_silu(av), dv, TN, precision=HI)
        da = _dot(dv[0:8, :], w_ref[...], NT, precision=HI) * _dsilu(av[0:8, :])
        row = lax.broadcasted_iota(jnp.int32, da.shape, 0)
        dc_ref[...] = jnp.where(row == 0, da, 0.0)

    return pl.pallas_call(
        body, name="mod_bwd",
        out_shape=[jax.ShapeDtypeStruct(w.shape, F32), jax.ShapeDtypeStruct((8, w.shape[0]), F32)],
        compiler_params=pltpu.CompilerParams(vmem_limit_bytes=VMEM_LIMIT),
    )(a, d, w)


def _sum_devices(g):
    def body(g_ref, o_ref):
        acc = g_ref[0]
        for i in range(1, g.shape[0]):
            acc = acc + g_ref[i]
        o_ref[...] = acc

    return pl.pallas_call(body, name="sum_devices_%d" % g.shape[1],
                          out_shape=jax.ShapeDtypeStruct(g.shape[1:], F32))(g)


def _sum_windows(g, name, col0, prev):
    n, r, c = g.shape
    tr = 128
    first = col0 // tr

    def body(g_ref, *refs):
        o_ref = refs[-1]
        x, y, cc = _place()
        lane0 = (W_IN_SHARD * _slot((x, y, cc))) % DH
        acc = g_ref[0].astype(F32)
        for i in range(1, n):
            acc = acc + g_ref[i].astype(F32)
        o_ref[...] = pltpu.roll(acc, (c - lane0) % c, 1).T

    held = [] if prev is None else [prev]
    return pl.pallas_call(
        body, name=name, grid=(r // tr,),
        in_specs=[pl.BlockSpec((n, tr, c), lambda i: (0, i, 0))] + [pl.BlockSpec(memory_space=pl.ANY)] * len(held),
        out_specs=pl.BlockSpec((c, tr), lambda i: (0, first + i)),
        out_shape=jax.ShapeDtypeStruct((c, D), F32),
        input_output_aliases={1: 0} if held else {},
        compiler_params=_cp("parallel"),
    )(g, *held)


def _adam_rows(r, c, n):
    budget = 6 * 1024 * 1024
    best = None
    for tr in range(16, r + 1, 16):
        if r % tr == 0 and tr * c * (2 * n + 28) <= budget:
            best = tr
    return best if best is not None else r


def _adamw(g, w, m, v, name):
    n, r, c = g.shape
    tr = _adam_rows(r, c, n)
    bc1 = 1.0 - ADAM_B1 ** ADAM_STEP
    bc2 = 1.0 - ADAM_B2 ** ADAM_STEP

    def body(g_ref, w_ref, m_ref, v_ref, go_ref, d_ref, mo_ref, vo_ref):
        grad = g_ref[0].astype(F32)
        for i in range(1, n):
            grad = grad + g_ref[i].astype(F32)
        go_ref[...] = grad
        m_new = ADAM_B1 * m_ref[...] + (1.0 - ADAM_B1) * grad
        v_new = ADAM_B2 * v_ref[...] + (1.0 - ADAM_B2) * (grad * grad)
        mo_ref[...] = m_new
        vo_ref[...] = v_new
        d_ref[...] = -ADAM_LR * ((m_new / bc1) / (jnp.sqrt(v_new / bc2) + ADAM_EPS) + ADAM_WD * w_ref[...])

    blk = pl.BlockSpec((tr, c), lambda i: (i, 0))
    out = jax.ShapeDtypeStruct((r, c), F32)
    return pl.pallas_call(
        body, name=name, grid=(r // tr,),
        in_specs=[pl.BlockSpec((n, tr, c), lambda i: (0, i, 0)), blk, blk, blk],
        out_specs=[blk] * 4, out_shape=[out] * 4,
        compiler_params=_cp("parallel"),
    )(g, w, m, v)


ADAM_ROWS3 = 168


def _adam_math(grad, w, m, v):
    bc1 = 1.0 - ADAM_B1 ** ADAM_STEP
    bc2 = 1.0 - ADAM_B2 ** ADAM_STEP
    m_new = ADAM_B1 * m + (1.0 - ADAM_B1) * grad
    v_new = ADAM_B2 * v + (1.0 - ADAM_B2) * (grad * grad)
    delta = -ADAM_LR * ((m_new / bc1) / (jnp.sqrt(v_new / bc2) + ADAM_EPS) + ADAM_WD * w)
    return delta, m_new, v_new


def _adamw_rows3(g, w3, m3, v3, name):
    r, _, c = w3.shape
    n = ADAM_ROWS3
    starts = list(range(0, r - n, n)) + [r - n]

    def body(g_hbm, w_hbm, m_hbm, v_hbm, go_hbm, d_hbm, mo_hbm, vo_hbm, gbuf, ibuf, obuf, in_sems, out_sems):
        def fetch(p):
            r0, slot = starts[p], p % 2
            g0 = (r0 // 8) * 8
            cps = [pltpu.make_async_copy(g_hbm.at[pl.ds(g0, n + 8)], gbuf.at[slot], in_sems.at[slot, 0])]
            cps += [pltpu.make_async_copy(h.at[pl.ds(r0, n), 0], ibuf.at[slot, k], in_sems.at[slot, 1 + k])
                    for k, h in enumerate((w_hbm, m_hbm, v_hbm))]
            for cp in cps:
                cp.start()
            return cps

        pending, outs = fetch(0), []
        for p, r0 in enumerate(starts):
            slot = p % 2
            nxt = fetch(p + 1) if p + 1 < len(starts) else []
            for cp in pending:
                cp.wait()
            grad = gbuf[slot, pl.ds(r0 - (r0 // 8) * 8, n), :]
            delta, m_new, v_new = _adam_math(grad, ibuf[slot, 0], ibuf[slot, 1], ibuf[slot, 2])
            for cp in outs:
                cp.wait()
            for k, val in enumerate((grad, delta, m_new, v_new)):
                obuf[slot, k] = val
            outs = [pltpu.make_async_copy(obuf.at[slot, k], h.at[pl.ds(r0, n), 0], out_sems.at[slot, k])
                    for k, h in enumerate((go_hbm, d_hbm, mo_hbm, vo_hbm))]
            for cp in outs:
                cp.start()
            pending = nxt
        for cp in outs:
            cp.wait()

    out = jax.ShapeDtypeStruct(w3.shape, F32)
    return pl.pallas_call(
        body, name=name, in_specs=[ANY] * 4, out_specs=[ANY] * 4, out_shape=[out] * 4,
        scratch_shapes=[pltpu.VMEM((2, n + 8, c), F32), pltpu.VMEM((2, 3, n, c), F32), pltpu.VMEM((2, 4, n, c), F32),
                        pltpu.SemaphoreType.DMA((2, 4)), pltpu.SemaphoreType.DMA((2, 4))],
        compiler_params=pltpu.CompilerParams(vmem_limit_bytes=VMEM_LIMIT),
    )(g, w3, m3, v3)


def kernel(x, c, ctx, c_ctx, w_mod, b_mod, norm_pre1, norm_post1, norm_pre2, norm_post2, w_in, hg_lb, hg_onorm, gla_w_gk, gla_b_gk, gla_onorm, w_br_hg, w_br_gla, w_out, w_ff_gate, w_ff_up, w_ff_down, loss_target, m_c_ctx, m_w_mod, m_b_mod, m_norm_pre1, m_norm_post1, m_norm_pre2, m_norm_post2, m_w_in, m_hg_lb, m_hg_onorm, m_gla_w_gk, m_gla_b_gk, m_gla_onorm, m_w_br_hg, m_w_br_gla, m_w_out, m_w_ff_gate, m_w_ff_up, m_w_ff_down, v_c_ctx, v_w_mod, v_b_mod, v_norm_pre1, v_norm_post1, v_norm_pre2, v_norm_post2, v_w_in, v_hg_lb, v_hg_onorm, v_gla_w_gk, v_gla_b_gk, v_gla_onorm, v_w_br_hg, v_w_br_gla, v_w_out, v_w_ff_gate, v_w_ff_up, v_w_ff_down):
    xi, yi, ci = lax.axis_index("x"), lax.axis_index("y"), lax.axis_index("c")
    me = 4 * xi + 2 * yi + ci
    t = CTX + x.shape[1]

    w_in_pieces, w_in_state = [], {}

    def w_in_piece(i):
        return (_view_near_rows((i * W_IN_PIECE, (i + 1) * W_IN_PIECE)), w_in_state["src"], w_in_state["land"])

    def started_w_in(handle):
        w_in_state.update(src=handle["srcs"], land=handle["lands"])
        w_in_pieces.append(handle)

    tr_ = lambda a: jnp.swapaxes(a[0], 0, 1)
    w_in_bf = jnp.pad(w_in[0].astype(BF16), ((0, 0), (0, W_IN_PAD - W_IN_SHARD)))
    w_in_state.update(src=[w_in_bf], land=[lax.empty((N_DEV,) + w_in_bf.shape, BF16)])
    gathered = lambda arrs: [(N_DEV,) + a.shape for a in arrs]
    whole = lambda arrs: (_view_whole, arrs, gathered(arrs))
    small_in = [c, hg_lb, gla_w_gk[0], gla_b_gk[0]]
    (small_handle, piece), tok = _split_start([whole(small_in), w_in_piece(0)], "ag_small_start", c)
    started_w_in(piece)
    c_all, lb_g, wgk_g, bgk_g = _split_wait(small_handle, "ag_small_wait", tok)
    big = [w_in[0], w_br_hg[0], w_br_gla[0], w_out[0], tr_(w_ff_gate), tr_(w_ff_up), w_ff_down[0]]
    big_bf = [None] + [w.astype(BF16) for w in big[1:]]
    cols = lambda g: jnp.transpose(g, (1, 0, 2)).reshape(g.shape[1], N_DEV * g.shape[2])

    def get_w_in(after):
        w_full, first = None, 0
        for s, last in enumerate(W_IN_STAGES):
            for i in range(first, last):
                land = _split_wait(w_in_pieces[i], "ag_w_in_wait%d" % i, after if w_full is None else [after, w_full],
                                   srcs=w_in_state["src"], lands=w_in_state["land"])
                w_in_state.update(src=w_in_pieces[i]["srcs"], land=land)
            rows = (first * W_IN_PIECE, last * W_IN_PIECE)
            w_in_state["land"] = [_forward_to_sibling(w_in_state["land"][0], "ag_w_in_forward%d" % s, rows)]
            w_full = _assemble_w_in(w_in_state["land"][0], rows, w_full, "assemble_w_in%d" % s)
            first = last
        return w_full

    def get_mix(after):
        g_brh, g_brg, g_out = _split_wait(mix_handle, "ag_mix_wait", after)
        return _gate_cols(cols(g_brh)), _gate_cols(cols(g_brg)), _gate_rows(g_out.reshape(D, D))

    def get_ffn(after):
        g_gate, g_up, g_down = _split_wait(ffn_handle, "ag_ffn_wait", after)
        return (g_gate.reshape(D_FF, D), g_up.reshape(D_FF, D)), g_down.reshape(D_FF, D)

    hg_lb_full = jnp.transpose(lb_g, (1, 2, 0, 3)).reshape(2, 2, HW)
    wgk_k = _layout_wgk(jnp.transpose(wgk_g, (1, 2, 0, 3)).reshape(2, 16, HW)).astype(BF16)
    bgk_k = jnp.transpose(bgk_g, (1, 0, 2)).reshape(1, D)
    onw = jnp.concatenate([jnp.tile(hg_onorm, (1, NH // 2)), jnp.tile(gla_onorm, (1, NH // 2))], axis=1)

    n_mod = w_mod.shape[2]
    a9 = jnp.concatenate([c_ctx[None], c_all[:, 0], jnp.zeros((16 - 1 - N_DEV, D), F32)], axis=0)
    b_loc = lax.dynamic_slice(b_mod, (0, me * n_mod), (1, n_mod))
    s_loc = _mod_fwd(a9, w_mod[0], b_loc)
    (mod_handle, piece), tok = _split_start([whole([s_loc]), w_in_piece(1)], "ag_mod_start", s_loc)
    started_w_in(piece)
    for i in range(2, D // W_IN_PIECE):
        (piece,), tok = _split_start([w_in_piece(i)], "ag_w_in_start%d" % i, tok)
        started_w_in(piece)
    s_all, = _split_wait(mod_handle, "ag_mod_wait", tok)
    mod_all = jnp.transpose(s_all, (1, 0, 2)).reshape(16, N_DEV * n_mod)
    pad8 = lambda m: jnp.concatenate([m.reshape(6, D), jnp.zeros((2, D), F32)], axis=0)
    modc = pad8(mod_all[0])
    modx = pad8(lax.dynamic_slice(mod_all, (1 + me, 0), (1, N_DEV * n_mod))[0])

    (mix_handle, ffn_handle), tok = _split_start([whole(big_bf[1:4]), whole(big_bf[4:])], "ag_big_start", s_all)

    z = (ctx[0], x[0])
    modx = _tie(modx, tok, "tie_mod")
    norms = (norm_pre1, norm_post1, norm_pre2, norm_post2)
    shard = lambda d: jnp.transpose(d.reshape(d.shape[0], N_DEV, -1), (1, 0, 2)).astype(BF16)
    rowshard = lambda d: d.reshape(N_DEV, d.shape[0] // N_DEV, d.shape[1]).astype(BF16)
    sent, w_in_grad = [], {}

    def w_in_chunk(i):
        half, rows = W_IN_GRAD_CHUNKS[i]
        return (_view_window(rows), w_in_grad[half], [(N_DEV, rows[1] - rows[0], D)])

    def sent_w_in(i, handle):
        w_in_grad[W_IN_GRAD_CHUNKS[i][0]] = handle["srcs"]
        sent.append(("w_in%d" % i, ["w_in#%d" % i], handle))

    def send(names, grads, x_after):
        if names == ("w_in_a",):
            w_in_grad["a"] = list(grads)
            (handle,), tok = _split_start([w_in_chunk(0)], "grads_w_in0_start", x_after)
            sent_w_in(0, handle)
            return _tie(x_after, tok, "tie_w_in0")
        if names == ("w_in_b",):
            w_in_grad["b"] = list(grads)
            return x_after
        arrs, leaves = [], []
        for nm, g in zip(names, grads):
            if nm in ("w_gate_t", "w_up_t"):
                arrs.append(rowshard(g))
                leaves.append({"w_gate_t": "w_ff_gate", "w_up_t": "w_ff_up"}[nm])
            elif nm == "w_down":
                arrs.append(rowshard(g))
                leaves.append("w_ff_down")
            elif nm == "w_out":
                arrs.append(rowshard(g[GOFF:GOFF + D]))
                leaves.append(nm)
            else:
                arrs.append(shard(g[:, GOFF:GOFF + D]))
                leaves.append(nm)
        (handle,), tok = _split_start([(_view_block, arrs, [a.shape for a in arrs])], "grads_%s_start" % names[0],
                                      x_after)
        sent.append((names[0], leaves, handle))
        return _tie(x_after, tok, "tie_" + names[0])

    r = _local_step(z, loss_target[0], modc, modx, norms, onw, hg_lb_full, wgk_k, bgk_k,
                    get_w_in, get_mix, get_ffn, send)
    grad_x = r["grad_x"][None]

    sm_pre, sm_mid, sm_fin = r["sm_pre"], r["sm_mid"], r["sm_final"]
    dmodc = jnp.stack([sm_pre[0], sm_pre[2], sm_mid[4], sm_mid[0], sm_mid[2], sm_fin[0]]).reshape(-1)
    dmodx = jnp.stack([sm_pre[1], sm_pre[3], sm_mid[5], sm_mid[1], sm_mid[3], sm_fin[1]]).reshape(-1)
    on = r["sm_post"][0].reshape(NH, DH)
    pieces = [dmodc, dmodx, sm_pre[4], sm_mid[7], sm_mid[6], sm_fin[2], on[:NH // 2].sum(0), on[NH // 2:].sum(0),
              r["d_lb"][:2].reshape(-1), _unlayout_wgk(r["d_wgk"]).reshape(-1), r["d_bgk"][0]]
    loss_local = (0.5 / D) * jnp.sum(r["loss_vec"])
    pieces.append(jnp.concatenate([loss_local.reshape(1), jnp.zeros((DH - 1,), F32)]))
    sizes = [p.shape[0] for p in pieces]
    pack = jnp.concatenate(pieces).reshape(-1, DH)
    moms = [(m_w_in, v_w_in), (m_w_br_hg, v_w_br_hg), (m_w_br_gla, v_w_br_gla), (m_w_out, v_w_out),
            (m_w_ff_gate, v_w_ff_gate), (m_w_ff_up, v_w_ff_up), (m_w_ff_down, v_w_ff_down)]
    names = ["w_in", "w_br_hg", "w_br_gla", "w_out", "w_ff_gate", "w_ff_up", "w_ff_down"]
    wmv = {nm: (w, m, v) for nm, w, (m, v) in zip(names, big, moms)}
    res = {}

    def update(nm):
        w, m, v = wmv[nm]
        if nm in ("w_ff_gate", "w_ff_up"):
            outs = _adamw(recv[nm], w, tr_(m), tr_(v), "adamw_" + nm)
            res[nm] = [jnp.swapaxes(o, 0, 1)[None] for o in outs]
        else:
            res[nm] = [o[None] for o in _adamw(recv[nm], w, m[0], v[0], "adamw_" + nm)]

    (small_handle, handle), tok = _split_start([whole([pack]), w_in_chunk(1)], "small_grads_start", pack)
    sent_w_in(1, handle)
    recv = {}
    for first, leaves, handle in sent:
        if not first.startswith("w_in"):
            recv.update(zip(leaves, _split_wait(handle, "grads_%s_wait" % first, tok)))
    update("w_ff_gate")
    update("w_ff_up")
    pack_all, = _split_wait(small_handle, "small_grads_wait", [res["w_ff_gate"][0], res["w_ff_up"][0]])
    tot = _sum_devices(pack_all).reshape(-1)
    offs = [sum(sizes[:i]) for i in range(len(sizes))]
    part = lambda i: tot[offs[i]:offs[i] + sizes[i]]
    dmodc_t, dmodx_t = part(0), part(1)
    g_b_mod = (dmodc_t + dmodx_t)[None]
    g_norms = [part(i)[None] for i in (2, 3, 4, 5)]
    g_hg_on, g_gla_on = part(6)[None], part(7)[None]
    lb0 = lax.dynamic_slice(part(8).reshape(2, HW), (0, me * (HW // N_DEV)), (2, HW // N_DEV))
    g_hg_lb = jnp.stack([lb0, -lb0])
    g_wgk = lax.dynamic_slice(part(9).reshape(2, 16, HW), (0, 0, me * (HW // N_DEV)), (2, 16, HW // N_DEV))[None]
    g_bgk = lax.dynamic_slice(part(10).reshape(2, HW), (0, me * (HW // N_DEV)), (2, HW // N_DEV))[None]
    loss = part(11)[0]

    dmx_all = pack_all.reshape(N_DEV, -1)[:, sizes[0]:sizes[0] + sizes[1]]
    d9 = jnp.concatenate([lax.dynamic_slice(dmodc_t[None], (0, me * n_mod), (1, n_mod)),
                          lax.dynamic_slice(dmx_all, (0, me * n_mod), (N_DEV, n_mod)),
                          jnp.zeros((16 - 1 - N_DEV, n_mod), F32)], axis=0)
    g_w_mod, dcc_part = _mod_bwd(a9, d9, w_mod[0])
    (cctx_handle, handle), tok = _split_start([whole([dcc_part]), w_in_chunk(2)], "c_ctx_start", dcc_part)
    sent_w_in(2, handle)
    recv["w_ff_down"] = _tie(recv["w_ff_down"], tok, "tie_down")
    update("w_ff_down")
    res["w_mod"] = [o[None] for o in _adamw(g_w_mod[None], w_mod[0], m_w_mod[0], v_w_mod[0], "adamw_w_mod")]
    for nm in ("w_out", "w_br_hg", "w_br_gla"):
        update(nm)
    dcc_all, = _split_wait(cctx_handle, "c_ctx_wait", [res["w_ff_down"][0], res["w_mod"][0]])
    g_c_ctx = _sum_devices(dcc_all)[0]

    small = [("c_ctx", c_ctx, m_c_ctx, v_c_ctx, g_c_ctx), ("b_mod", b_mod, m_b_mod, v_b_mod, g_b_mod),
             ("norm_pre1", norm_pre1, m_norm_pre1, v_norm_pre1, g_norms[0]),
             ("norm_post1", norm_post1, m_norm_post1, v_norm_post1, g_norms[1]),
             ("norm_pre2", norm_pre2, m_norm_pre2, v_norm_pre2, g_norms[2]),
             ("norm_post2", norm_post2, m_norm_post2, v_norm_post2, g_norms[3]),
             ("hg_lb", hg_lb, m_hg_lb, v_hg_lb, g_hg_lb), ("hg_onorm", hg_onorm, m_hg_onorm, v_hg_onorm, g_hg_on),
             ("gla_w_gk", gla_w_gk, m_gla_w_gk, v_gla_w_gk, g_wgk), ("gla_b_gk", gla_b_gk, m_gla_b_gk, v_gla_b_gk, g_bgk),
             ("gla_onorm", gla_onorm, m_gla_onorm, v_gla_onorm, g_gla_on)]
    flat = lambda k: jnp.concatenate([s[k].reshape(-1) for s in small]).reshape(-1, DH)
    outs = _adamw(flat(4)[None], flat(1), flat(2), flat(3), "adamw_small")
    off = 0
    for nm, w, _, _, _ in small:
        res[nm] = [o.reshape(-1)[off:off + w.size].reshape(w.shape) for o in outs]
        off += w.size

    done = [res[nm][0] for nm in names[1:]] + [res["w_mod"][0]] + [o for nm, *_ in small for o in res[nm]]
    g_w_in, col0 = None, 0
    for i, (first, leaves, handle) in enumerate(s for s in sent if s[0].startswith("w_in")):
        half = W_IN_GRAD_CHUNKS[i][0]
        land, = _split_wait(handle, "grads_%s_wait" % first, done, srcs=w_in_grad[half])
        w_in_grad[half] = handle["srcs"]
        g_w_in = _sum_windows(land, "sum_windows%d" % i, col0, g_w_in)
        col0 += land.shape[1]
    major = lambda a: jnp.transpose(a, (2, 0, 1))
    outs = _adamw_rows3(g_w_in, major(w_in), major(m_w_in), major(v_w_in), "adamw_w_in")
    res["w_in"] = [jnp.transpose(o, (1, 2, 0)) for o in outs]

    order = ["c_ctx", "w_mod", "b_mod", "norm_pre1", "norm_post1", "norm_pre2", "norm_post2", "w_in", "hg_lb",
             "hg_onorm", "gla_w_gk", "gla_b_gk", "gla_onorm", "w_br_hg", "w_br_gla", "w_out", "w_ff_gate", "w_ff_up",
             "w_ff_down"]
    return (loss, grad_x, *[res[n][k] for k in range(4) for n in order])
```

```python
import functools

import jax
import jax.numpy as jnp
from jax import lax
from jax.experimental import pallas as pl
from jax.experimental.pallas import tpu as pltpu

F32 = jnp.float32
BF16 = jnp.bfloat16
HI = lax.Precision.HIGHEST

N_DEV = 8
D = 1024
CTX = 256
HW = 512
DH = 128
NH = 8
D_FF = 2816
EPS = 1e-6
GLA_NORM = 16.0
CHUNK = 64
TR = 256
NCT = CTX // TR
W_IN_COLS = 7168
MAIN0 = 0
LR0 = 4608
GW = 1152
GOFF = 32
GATE_HG0 = LR0
GATE_GLA0 = LR0 + D
LEVELS = (32, 16, 8)
EXP_CLAMP = 80.0
VMEM_LIMIT = 48 * 1024 * 1024

ADAM_LR, ADAM_B1, ADAM_B2, ADAM_EPS, ADAM_WD, ADAM_STEP = 0.001, 0.9, 0.999, 1e-08, 0.01, 10


def _cp(*sem):
    return pltpu.CompilerParams(dimension_semantics=sem, vmem_limit_bytes=VMEM_LIMIT)


def _sig(x):
    return jax.nn.sigmoid(x)


def _silu(x):
    return x * _sig(x)


def _dsilu(x):
    s = _sig(x)
    return s * (1.0 + x * (1.0 - s))


def _rstd(x):
    return lax.rsqrt(jnp.mean(x * x, axis=-1, keepdims=True) + EPS)


def _rms_bwd(a, y, r):
    return r * (a - y * (r * r) * jnp.mean(a * y, axis=-1, keepdims=True))


def _colsum(x):
    return jnp.sum(x, axis=0, keepdims=True)


def _dot(a, b, dims, precision=None):
    return lax.dot_general(a, b, (dims, ((), ())), preferred_element_type=F32, precision=precision)


NN = ((1,), (0,))
NT = ((1,), (1,))
TN = ((0,), (0,))

SCAN_HEADS_FWD = 4
SCAN_HEADS_BWD = 4


def _split_dot(m, x):
    mb = m.astype(BF16)
    x1 = x.astype(BF16)
    r1 = x - x1.astype(F32)
    x2 = r1.astype(BF16)
    x3 = (r1 - x2.astype(F32)).astype(BF16)
    return _dot(mb, x1, NN) + _dot(mb, x2, NN) + _dot(mb, x3, NN)


def _matmul(a, b, dims, out_dtype, name, tm, tn, tk, a_off=0, m_out=None):
    a_pair = isinstance(a, (tuple, list))
    as_ = list(a) if a_pair else [a]
    a = as_[0]
    pair = isinstance(b, (tuple, list))
    bs = list(b) if pair else [b]
    b1 = bs[0]
    rows = b1.shape[0] * len(bs)
    half = None
    if dims == NN:
        m, k, n = a.shape[0], rows, b1.shape[1]
        a_spec = pl.BlockSpec((tm, tk), lambda i, j, kk: (i, kk + a_off))
        half = b1.shape[0] // tk
        if a_pair:
            assert pair and a.shape[1] == b1.shape[0] and a_off == 0
            a_spec = [pl.BlockSpec((tm, tk), lambda i, j, kk: (i, jnp.minimum(kk, half - 1))),
                      pl.BlockSpec((tm, tk), lambda i, j, kk: (i, jnp.maximum(kk - half, 0)))]
        b_maps = [lambda i, j, kk: (kk, j)] if not pair else [
            lambda i, j, kk: (jnp.minimum(kk, half - 1), j), lambda i, j, kk: (jnp.maximum(kk - half, 0), j)]
        b_specs = [pl.BlockSpec((tk, tn), f) for f in b_maps]
        axis = 2
    elif dims == NT:
        m, k, n = a.shape[0], b1.shape[1], rows
        a_spec = pl.BlockSpec((tm, tk), lambda i, j, kk: (i, kk + a_off))
        half = b1.shape[0] // tn
        b_maps = [lambda i, j, kk: (j, kk)] if not pair else [
            lambda i, j, kk: (jnp.minimum(j, half - 1), kk), lambda i, j, kk: (jnp.maximum(j - half, 0), kk)]
        b_specs = [pl.BlockSpec((tn, tk), f) for f in b_maps]
        axis = 1
    else:
        assert not pair
        m, k = (a.shape[1] if m_out is None else m_out), a.shape[0]
        n = b1.shape[1]
        a_spec = pl.BlockSpec((tk, tm), lambda i, j, kk: (kk, i + a_off))
        b_specs = [pl.BlockSpec((tk, tn), lambda i, j, kk: (kk, j))]
    assert m % tm == 0 and n % tn == 0 and k % tk == 0, (name, m, n, k, tm, tn, tk)
    nk = k // tk
    nb = len(bs)
    na = len(as_)
    assert na == 1 or dims == NN

    def body(*refs):
        a_refs, refs = refs[:na], refs[na:]
        o_ref = refs[nb]
        if pair:
            bv = jnp.where(pl.program_id(axis) < half, refs[0][...], refs[1][...])
        else:
            bv = refs[0][...]
        av = a_refs[0][...] if na == 1 else jnp.where(pl.program_id(2) < half, a_refs[0][...], a_refs[1][...])
        part = _dot(av, bv, dims)
        if nk == 1:
            o_ref[...] = part.astype(o_ref.dtype)
            return
        acc_ref = refs[nb + 1]
        kk = pl.program_id(2)

        @pl.when(kk == 0)
        def _():
            acc_ref[...] = part

        @pl.when(kk > 0)
        def _():
            acc_ref[...] += part

        @pl.when(kk == nk - 1)
        def _():
            o_ref[...] = acc_ref[...].astype(o_ref.dtype)

    return pl.pallas_call(
        body,
        name=name,
        grid=(m // tm, n // tn, nk),
        in_specs=(a_spec if a_pair else [a_spec]) + b_specs,
        out_specs=pl.BlockSpec((tm, tn), lambda i, j, kk: (i, j)),
        out_shape=jax.ShapeDtypeStruct((m, n), out_dtype),
        scratch_shapes=[] if nk == 1 else [pltpu.VMEM((tm, tn), F32)],
        compiler_params=_cp("parallel", "parallel", "arbitrary"),
    )(*as_, *bs)


def _mm_gu_act(h, w_gate_t, w_up_t, name, tm):
    t = h.shape[0]
    tn = D_FF // 2

    def body(a_ref, bg_ref, bu_ref, u_ref, v_ref, act_ref):
        a = a_ref[...]
        u = _dot(a, bg_ref[...], NT)
        v = _dot(a, bu_ref[...], NT)
        u_ref[...] = u.astype(BF16)
        v_ref[...] = v.astype(BF16)
        act_ref[...] = (_silu(u) * v).astype(BF16)

    wspec = pl.BlockSpec((tn, D), lambda i, j: (j, 0))
    ospec = pl.BlockSpec((tm, tn), lambda i, j: (i, j))
    out = jax.ShapeDtypeStruct((t, D_FF), BF16)
    return pl.pallas_call(
        body, name=name, grid=(t // tm, D_FF // tn),
        in_specs=[pl.BlockSpec((tm, D), lambda i, j: (i, 0)), wspec, wspec],
        out_specs=[ospec] * 3, out_shape=[out] * 3,
        compiler_params=_cp("parallel", "parallel"),
    )(h, w_gate_t, w_up_t)


def _mm_down_dx_act(dy, w_down, u, v, name, tm):
    t = dy.shape[0]
    tn = D_FF // 2

    def body(a_ref, b_ref, u_ref, v_ref, du_ref, dv_ref):
        dact = _dot(a_ref[...], b_ref[...], NT)
        u = u_ref[...].astype(F32)
        du_ref[...] = (dact * v_ref[...].astype(F32) * _dsilu(u)).astype(BF16)
        dv_ref[...] = (dact * _silu(u)).astype(BF16)

    ospec = pl.BlockSpec((tm, tn), lambda i, j: (i, j))
    out = jax.ShapeDtypeStruct((t, D_FF), BF16)
    return pl.pallas_call(
        body, name=name, grid=(t // tm, D_FF // tn),
        in_specs=[pl.BlockSpec((tm, D), lambda i, j: (i, 0)), pl.BlockSpec((tn, D), lambda i, j: (j, 0)), ospec, ospec],
        out_specs=[ospec] * 2, out_shape=[out] * 2,
        compiler_params=_cp("parallel", "parallel"),
    )(dy, w_down, u, v)


def _row(c):
    return pl.BlockSpec((TR, c), lambda i: (i, 0))


def _rowcol(width, cb):
    return pl.BlockSpec((TR, width), lambda i: (i, cb))


def _full(shape):
    return pl.BlockSpec(shape, lambda i: (0,) * len(shape))


def _mod_row(mc_ref, mx_ref, k, is_ctx):
    return jnp.where(is_ctx, mc_ref[k:k + 1, :], mx_ref[k:k + 1, :])


def _z_specs():
    return [pl.BlockSpec((TR, D), lambda i: (jnp.minimum(i, NCT - 1), 0)),
            pl.BlockSpec((TR, D), lambda i: (jnp.maximum(i - NCT, 0), 0))]


def _z_tile(c_ref, x_ref, is_ctx):
    return jnp.where(is_ctx, c_ref[...], x_ref[...])


def _acc_row(ref, k, val):
    ref[k:k + 1, :] += val


def _acc_mod(ref, k, is_ctx, val):
    zero = jnp.zeros_like(val)
    ref[k:k + 1, :] += jnp.where(is_ctx, val, zero)
    ref[k + 1:k + 2, :] += jnp.where(is_ctx, zero, val)


def _prenorm(z, nw, modc, modx, i_shift, i_scale, name):
    t = z[0].shape[0] + z[1].shape[0]

    def body(zc_ref, zx_ref, nw_ref, mc_ref, mx_ref, h_ref):
        is_ctx = pl.program_id(0) < NCT
        x = _z_tile(zc_ref, zx_ref, is_ctx)
        n = x * _rstd(x) * nw_ref[...]
        h = n * (1.0 + _mod_row(mc_ref, mx_ref, i_scale, is_ctx)) + _mod_row(mc_ref, mx_ref, i_shift, is_ctx)
        h_ref[...] = h.astype(BF16)

    return pl.pallas_call(
        body, name=name, grid=(t // TR,),
        in_specs=_z_specs() + [_full((1, D)), _full((8, D)), _full((8, D))],
        out_specs=_row(D),
        out_shape=jax.ShapeDtypeStruct((t, D), BF16),
        compiler_params=_cp("parallel"),
    )(*z, nw, modc, modx)


def _hg_lb(lb_ref, d):
    a0 = lb_ref[0, d:d + 1, :]
    a1 = lb_ref[1, d:d + 1, :]
    mx = jnp.maximum(a0, a1)
    e0 = jnp.exp(a0 - mx)
    e1 = jnp.exp(a1 - mx)
    return e0 / (e0 + e1)


def _log_sigmoid(x):
    return jnp.minimum(x, 0.0) - jnp.log(1.0 + jnp.exp(-jnp.abs(x)))


def _gates_fwd(p, hg_lb, wgk, bgk):
    t = p.shape[0]
    seg = lambda j: _rowcol(HW, MAIN0 // HW + j)

    def body(hq_ref, hi_ref, hf_ref, hb_ref, gq_ref, gk_ref, gv_ref, lr_ref, lb_ref, wgk_ref, bgk_ref,
             q_ref, v_ref, kf_ref, kb_ref, gf_ref, gb_ref):
        q_ref[:, :HW] = _silu(hq_ref[...].astype(F32)).astype(BF16)
        q_ref[:, HW:] = (gq_ref[...].astype(F32) * (DH ** -0.5)).astype(BF16)
        v_ref[:, :HW] = hi_ref[...]
        v_ref[:, HW:] = gv_ref[...]
        xg = _dot(lr_ref[...].astype(BF16), wgk_ref[...], NN) + bgk_ref[...]
        for d, (raw_ref, k_ref, g_ref) in enumerate(((hf_ref, kf_ref, gf_ref), (hb_ref, kb_ref, gb_ref))):
            lbd = _hg_lb(lb_ref, d)
            f = lbd + (1.0 - lbd) * _sig(raw_ref[...].astype(F32))
            k_ref[:, :HW] = (1.0 - f).astype(BF16)
            k_ref[:, HW:] = gk_ref[...]
            g_ref[:, :HW] = jnp.log(f)
            g_ref[:, HW:] = _log_sigmoid(xg[:, d * HW:(d + 1) * HW]) * (1.0 / GLA_NORM)

    out = jax.ShapeDtypeStruct((t, D), F32)
    outb = jax.ShapeDtypeStruct((t, D), BF16)
    return pl.pallas_call(
        body, name="gates_fwd", grid=(t // TR,),
        in_specs=[seg(0), seg(1), seg(2), seg(3), seg(5), seg(6), seg(7), _rowcol(DH, LR0 // DH),
                  _full((2, 2, HW)), _full((DH, D)), _full((1, D))],
        out_specs=[_row(D)] * 6,
        out_shape=[outb] * 4 + [out] * 2,
        compiler_params=_cp("parallel"),
    )(p, p, p, p, p, p, p, p, hg_lb, wgk, bgk)


def _post_fwd(o_fw, o_bw, p, onw):
    t = o_fw.shape[0]

    def body(of_ref, ob_ref, g1_ref, g2_ref, w_ref, y_ref):
        for h in range(NH):
            sl = slice(h * DH, (h + 1) * DH)
            o = of_ref[:, sl] + ob_ref[:, sl]
            g_ref = g1_ref if h < NH // 2 else g2_ref
            gs = slice((h % (NH // 2)) * DH, (h % (NH // 2) + 1) * DH)
            n = o * _rstd(o) * w_ref[:, sl]
            y_ref[:, sl] = (n * _silu(g_ref[:, gs].astype(F32))).astype(BF16)

    return pl.pallas_call(
        body, name="post_fwd", grid=(t // TR,),
        in_specs=[_row(D), _row(D), _rowcol(HW, MAIN0 // HW + 4), _rowcol(HW, MAIN0 // HW + 8), _full((1, D))],
        out_specs=_row(D),
        out_shape=jax.ShapeDtypeStruct((t, D), BF16),
        compiler_params=_cp("parallel"),
    )(o_fw, o_bw, p, p, onw)


def _gate_window_specs(col0):
    return [_rowcol(HW, col0 // HW), _rowcol(HW, col0 // HW + 1), _rowcol(DH, (col0 + 2 * HW) // DH)]


def _gate_window(refs):
    return jnp.concatenate([r[...].astype(F32) for r in refs], axis=1)


def _branch_merge(y, w_hg, w_gla, p):
    t = y.shape[0]

    def body(y_ref, wh_ref, wg_ref, a0, a1, a2, b0, b1, b2, u1_ref, u2_ref, m_ref):
        u1 = _dot(y_ref[:, :HW], wh_ref[...], NN)
        u2 = _dot(y_ref[:, HW:], wg_ref[...], NN)
        u1_ref[...] = u1.astype(BF16)
        u2_ref[...] = u2.astype(BF16)
        m_ref[...] = (_sig(_gate_window((a0, a1, a2))) * u1 + _sig(_gate_window((b0, b1, b2))) * u2).astype(BF16)

    out = jax.ShapeDtypeStruct((t, GW), BF16)
    return pl.pallas_call(
        body, name="branch_merge", grid=(t // TR,),
        in_specs=[_row(D), _full((HW, GW)), _full((HW, GW))] + _gate_window_specs(GATE_HG0)
        + _gate_window_specs(GATE_GLA0),
        out_specs=[_row(GW)] * 3, out_shape=[out] * 3,
        compiler_params=_cp("parallel"),
    )(y, w_hg, w_gla, p, p, p, p, p, p)


def _mid_fwd(z, y1, nw_post, nw_pre, modc, modx):
    t = y1.shape[0]

    def body(zc_ref, zx_ref, y_ref, wpo_ref, wpr_ref, mc_ref, mx_ref, z1_ref, h_ref):
        is_ctx = pl.program_id(0) < NCT
        y = y_ref[...].astype(F32)
        z1 = _z_tile(zc_ref, zx_ref, is_ctx) + _mod_row(mc_ref, mx_ref, 2, is_ctx) * (y * _rstd(y) * wpo_ref[...])
        z1_ref[...] = z1
        n = z1 * _rstd(z1) * wpr_ref[...]
        h = n * (1.0 + _mod_row(mc_ref, mx_ref, 4, is_ctx)) + _mod_row(mc_ref, mx_ref, 3, is_ctx)
        h_ref[...] = h.astype(BF16)

    return pl.pallas_call(
        body, name="mid_fwd", grid=(t // TR,),
        in_specs=_z_specs() + [_row(D), _full((1, D)), _full((1, D)), _full((8, D)), _full((8, D))],
        out_specs=[_row(D), _row(D)],
        out_shape=[jax.ShapeDtypeStruct((t, D), F32), jax.ShapeDtypeStruct((t, D), BF16)],
        compiler_params=_cp("parallel"),
    )(*z, y1, nw_post, nw_pre, modc, modx)


def _final(z1, y2, target, nw, modc, modx):
    t = z1.shape[0]

    def body(z1_ref, y_ref, tg_ref, w_ref, mc_ref, mx_ref, dz_ref, dy_ref, loss_ref, sm_ref):
        i = pl.program_id(0)
        is_ctx = i < NCT

        @pl.when(i == 0)
        def _():
            loss_ref[...] = jnp.zeros_like(loss_ref)
            sm_ref[...] = jnp.zeros_like(sm_ref)

        g = _mod_row(mc_ref, mx_ref, 5, is_ctx)
        y = y_ref[...].astype(F32)
        r = _rstd(y)
        w = w_ref[...]
        yr = y * r
        n = yr * w
        e = z1_ref[...] + g * n - tg_ref[...]
        lat = jnp.where(is_ctx, 0.0, 1.0)
        loss_ref[...] += lat * _colsum(e * e)
        dz = e * (lat / D)
        dz_ref[...] = dz
        _acc_mod(sm_ref, 0, is_ctx, _colsum(dz * n))
        dn = dz * g
        _acc_row(sm_ref, 2, _colsum(dn * yr))
        dy_ref[...] = _rms_bwd(dn * w, y, r).astype(BF16)

    return pl.pallas_call(
        body, name="final", grid=(t // TR,),
        in_specs=[_row(D), _row(D), pl.BlockSpec((TR, D), lambda i: (jnp.maximum(i - NCT, 0), 0)),
                  _full((1, D)), _full((8, D)), _full((8, D))],
        out_specs=[_row(D), _row(D), _full((1, D)), _full((8, D))],
        out_shape=[jax.ShapeDtypeStruct((t, D), F32), jax.ShapeDtypeStruct((t, D), BF16),
                   jax.ShapeDtypeStruct((1, D), F32), jax.ShapeDtypeStruct((8, D), F32)],
        compiler_params=_cp("arbitrary"),
    )(z1, y2, target, nw, modc, modx)


def _mid_bwd(dh2, dz, z1, y1, nw_post, nw_pre, modc, modx):
    t = z1.shape[0]

    def body(dh_ref, dz_ref, z1_ref, y_ref, wpo_ref, wpr_ref, mc_ref, mx_ref, dzo_ref, dy_ref, sm_ref):
        i = pl.program_id(0)
        is_ctx = i < NCT

        @pl.when(i == 0)
        def _():
            sm_ref[...] = jnp.zeros_like(sm_ref)

        dh = dh_ref[...].astype(F32)
        z1 = z1_ref[...]
        r = _rstd(z1)
        zr = z1 * r
        wpr = wpr_ref[...]
        n = zr * wpr
        _acc_mod(sm_ref, 0, is_ctx, _colsum(dh))
        _acc_mod(sm_ref, 2, is_ctx, _colsum(dh * n))
        dn = dh * (1.0 + _mod_row(mc_ref, mx_ref, 4, is_ctx))
        _acc_row(sm_ref, 6, _colsum(dn * zr))
        dz1 = dz_ref[...] + _rms_bwd(dn * wpr, z1, r)
        dzo_ref[...] = dz1
        y = y_ref[...].astype(F32)
        r1 = _rstd(y)
        yr = y * r1
        wpo = wpo_ref[...]
        g = _mod_row(mc_ref, mx_ref, 2, is_ctx)
        _acc_mod(sm_ref, 4, is_ctx, _colsum(dz1 * (yr * wpo)))
        dn1 = dz1 * g
        _acc_row(sm_ref, 7, _colsum(dn1 * yr))
        dy_ref[...] = _rms_bwd(dn1 * wpo, y, r1).astype(BF16)

    return pl.pallas_call(
        body, name="mid_bwd", grid=(t // TR,),
        in_specs=[_row(D)] * 4 + [_full((1, D)), _full((1, D)), _full((8, D)), _full((8, D))],
        out_specs=[_row(D), _row(D), _full((8, D))],
        out_shape=[jax.ShapeDtypeStruct((t, D), F32), jax.ShapeDtypeStruct((t, D), BF16),
                   jax.ShapeDtypeStruct((8, D), F32)],
        compiler_params=_cp("arbitrary"),
    )(dh2, dz, z1, y1, nw_post, nw_pre, modc, modx)


def _pre_bwd(dh1, dz, z, nw, modc, modx):
    t = dh1.shape[0]

    def body(dh_ref, dz_ref, zc_ref, zx_ref, w_ref, mc_ref, mx_ref, dzo_ref, sm_ref):
        i = pl.program_id(0)
        is_ctx = i < NCT

        @pl.when(i == 0)
        def _():
            sm_ref[...] = jnp.zeros_like(sm_ref)

        dh = dh_ref[...].astype(F32)
        x = _z_tile(zc_ref, zx_ref, is_ctx)
        r = _rstd(x)
        xr = x * r
        w = w_ref[...]
        _acc_mod(sm_ref, 0, is_ctx, _colsum(dh))
        _acc_mod(sm_ref, 2, is_ctx, _colsum(dh * (xr * w)))
        dn = dh * (1.0 + _mod_row(mc_ref, mx_ref, 1, is_ctx))
        _acc_row(sm_ref, 4, _colsum(dn * xr))
        dzo_ref[...] = dz_ref[...] + _rms_bwd(dn * w, x, r)

    return pl.pallas_call(
        body, name="pre_bwd", grid=(t // TR,),
        in_specs=[_row(D)] * 2 + _z_specs() + [_full((1, D)), _full((8, D)), _full((8, D))],
        out_specs=[pl.BlockSpec((TR, D), lambda i: (jnp.maximum(i - NCT, 0), 0)), _full((8, D))],
        out_shape=[jax.ShapeDtypeStruct((t - CTX, D), F32), jax.ShapeDtypeStruct((8, D), F32)],
        compiler_params=_cp("arbitrary"),
    )(dh1, dz, *z, nw, modc, modx)


def _branch_merge_bwd(dm, p, u1, u2, w_hg, w_gla):
    t = dm.shape[0]

    def body(dm_ref, a0, a1, a2, b0, b1, b2, u1_ref, u2_ref, wh_ref, wg_ref, du1_ref, du2_ref, dg_ref, dyh_ref, dyg_ref):
        dm_ = dm_ref[...].astype(F32)
        s1 = _sig(_gate_window((a0, a1, a2)))
        s2 = _sig(_gate_window((b0, b1, b2)))
        du1 = (dm_ * s1).astype(BF16)
        du2 = (dm_ * s2).astype(BF16)
        du1_ref[...] = du1
        du2_ref[...] = du2
        dg_ref[:, :GW] = (dm_ * u1_ref[...].astype(F32) * s1 * (1.0 - s1)).astype(BF16)
        dg_ref[:, GW:] = (dm_ * u2_ref[...].astype(F32) * s2 * (1.0 - s2)).astype(BF16)
        dyh_ref[...] = _dot(du1, wh_ref[...], NT).astype(BF16)
        dyg_ref[...] = _dot(du2, wg_ref[...], NT).astype(BF16)

    return pl.pallas_call(
        body, name="branch_merge_bwd", grid=(t // TR,),
        in_specs=[_row(GW)] + _gate_window_specs(GATE_HG0) + _gate_window_specs(GATE_GLA0)
        + [_row(GW), _row(GW), _full((HW, GW)), _full((HW, GW))],
        out_specs=[_row(GW), _row(GW), _row(2 * GW), _row(HW), _row(HW)],
        out_shape=[jax.ShapeDtypeStruct((t, GW), BF16), jax.ShapeDtypeStruct((t, GW), BF16),
                   jax.ShapeDtypeStruct((t, 2 * GW), BF16), jax.ShapeDtypeStruct((t, HW), BF16),
                   jax.ShapeDtypeStruct((t, HW), BF16)],
        compiler_params=_cp("parallel"),
    )(dm, p, p, p, p, p, p, u1, u2, w_hg, w_gla)


def _post_bwd(dy_hg, dy_gla, o_fw, o_bw, p, onw):
    t = o_fw.shape[0]

    def body(d1_ref, d2_ref, of_ref, ob_ref, g1_ref, g2_ref, w_ref, do_ref, dg_ref, sm_ref):
        @pl.when(pl.program_id(0) == 0)
        def _():
            sm_ref[...] = jnp.zeros_like(sm_ref)

        for h in range(NH):
            sl = slice(h * DH, (h + 1) * DH)
            gs = slice((h % (NH // 2)) * DH, (h % (NH // 2) + 1) * DH)
            g_ref, d_ref = (g1_ref, d1_ref) if h < NH // 2 else (g2_ref, d2_ref)
            o = of_ref[:, sl] + ob_ref[:, sl]
            r = _rstd(o)
            orr = o * r
            w = w_ref[:, sl]
            gt = g_ref[:, gs].astype(F32)
            dy = d_ref[:, gs].astype(F32)
            dg_ref[:, sl] = (dy * (orr * w) * _dsilu(gt)).astype(BF16)
            dn = dy * _silu(gt)
            sm_ref[0:1, sl] += _colsum(dn * orr)
            do_ref[:, sl] = _rms_bwd(dn * w, o, r)

    return pl.pallas_call(
        body, name="post_bwd", grid=(t // TR,),
        in_specs=[_row(HW), _row(HW), _row(D), _row(D), _rowcol(HW, MAIN0 // HW + 4), _rowcol(HW, MAIN0 // HW + 8),
                  _full((1, D))],
        out_specs=[_row(D), _row(D), _full((8, D))],
        out_shape=[jax.ShapeDtypeStruct((t, D), F32), jax.ShapeDtypeStruct((t, D), BF16),
                   jax.ShapeDtypeStruct((8, D), F32)],
        compiler_params=_cp("arbitrary"),
    )(dy_hg, dy_gla, o_fw, o_bw, p, p, onw)


def _gates_bwd(p, hg_lb, wgk, bgk, dgm, dgo, dq_f, dq_b, dv_f, dv_b, dk_f, dk_b, dg_f, dg_b):
    t = p.shape[0]
    seg = lambda j: _rowcol(HW, MAIN0 // HW + j)

    def body(hq_ref, hf_ref, hb_ref, lr_ref, lb_ref, wgk_ref, bgk_ref, dgm_ref, dgo_ref,
             dqf_ref, dqb_ref, dvf_ref, dvb_ref, dkf_ref, dkb_ref, dgf_ref, dgb_ref,
             dp_ref, dlb_ref, dw_ref, db_ref):
        @pl.when(pl.program_id(0) == 0)
        def _():
            dlb_ref[...] = jnp.zeros_like(dlb_ref)
            dw_ref[...] = jnp.zeros_like(dw_ref)
            db_ref[...] = jnp.zeros_like(db_ref)

        c0 = MAIN0

        def put(j, val):
            dp_ref[:, c0 + j * HW:c0 + (j + 1) * HW] = val.astype(BF16)

        dq = dqf_ref[...].astype(F32) + dqb_ref[...].astype(F32)
        dv = dvf_ref[...].astype(F32) + dvb_ref[...].astype(F32)
        put(0, dq[:, :HW] * _dsilu(hq_ref[...].astype(F32)))
        put(1, dv[:, :HW])
        put(5, dq[:, HW:] * (DH ** -0.5))
        put(7, dv[:, HW:])
        put(6, dkf_ref[:, HW:].astype(F32) + dkb_ref[:, HW:].astype(F32))
        dp_ref[:, c0 + 4 * HW:c0 + 5 * HW] = dgo_ref[:, :HW]
        dp_ref[:, c0 + 8 * HW:c0 + 9 * HW] = dgo_ref[:, HW:]
        lr = lr_ref[...].astype(BF16)
        xg = _dot(lr, wgk_ref[...], NN) + bgk_ref[...]
        dxg = []
        for d, (raw_ref, dk_ref, dg_ref) in enumerate(((hf_ref, dkf_ref, dgf_ref), (hb_ref, dkb_ref, dgb_ref))):
            lbd = _hg_lb(lb_ref, d)
            s = _sig(raw_ref[...].astype(F32))
            f = lbd + (1.0 - lbd) * s
            df = dg_ref[:, :HW] / f - dk_ref[:, :HW].astype(F32)
            put(2 + d, df * (1.0 - lbd) * s * (1.0 - s))
            dlb_ref[d:d + 1, :] += _colsum(df * (1.0 - s)) * (lbd * (1.0 - lbd))
            dxg.append(dg_ref[:, HW:] * (1.0 / GLA_NORM) * _sig(-xg[:, d * HW:(d + 1) * HW]))
        dxg = jnp.concatenate(dxg, axis=1)
        db_ref[0:1, :] += _colsum(dxg)
        dxg_b = dxg.astype(BF16)
        dw_ref[...] += _dot(lr, dxg_b, TN)
        dlr = _dot(dxg_b, wgk_ref[...], NT)
        dp_ref[:, LR0:LR0 + DH] = (dlr + dgm_ref[:, :DH].astype(F32)).astype(BF16)
        dp_ref[:, LR0 + DH:GATE_GLA0] = dgm_ref[:, DH:D]
        dp_ref[:, GATE_GLA0:GATE_GLA0 + DH] = dgm_ref[:, D:GW] + dgm_ref[:, GW:GW + DH]
        dp_ref[:, GATE_GLA0 + DH:GATE_GLA0 + GW] = dgm_ref[:, GW + DH:]
        dp_ref[:, GATE_GLA0 + GW:] = jnp.zeros((TR, W_IN_COLS - GATE_GLA0 - GW), BF16)

    return pl.pallas_call(
        body, name="gates_bwd", grid=(t // TR,),
        in_specs=[seg(0), seg(2), seg(3), _rowcol(DH, LR0 // DH), _full((2, 2, HW)), _full((DH, D)), _full((1, D)),
                  _row(2 * GW), _row(D)] + [_row(D)] * 8,
        out_specs=[_row(W_IN_COLS), _full((8, HW)), _full((DH, D)), _full((8, D))],
        out_shape=[jax.ShapeDtypeStruct((t, W_IN_COLS), BF16), jax.ShapeDtypeStruct((8, HW), F32),
                   jax.ShapeDtypeStruct((DH, D), F32), jax.ShapeDtypeStruct((8, D), F32)],
        compiler_params=_cp("arbitrary"),
    )(p, p, p, p, hg_lb, wgk, bgk, dgm, dgo, dq_f, dq_b, dv_f, dv_b, dk_f, dk_b, dg_f, dg_b)


def _scan_consts(rev):
    r = lax.broadcasted_iota(jnp.int32, (CHUNK, CHUNK), 0)
    u = lax.broadcasted_iota(jnp.int32, (CHUNK, CHUNK), 1)
    rp = lax.broadcasted_iota(jnp.int32, (CHUNK, 1), 0)
    if rev:
        r, u, rp = CHUNK - 1 - r, CHUNK - 1 - u, CHUNK - 1 - rp
    tri = jnp.where(u <= r, 1.0, 0.0).astype(F32)
    tri_t = jnp.where(r <= u, 1.0, 0.0).astype(F32)
    lv = []
    for b in LEVELS:
        sh = b.bit_length() - 1
        pair = ((r >> sh) == (u >> sh) + 1) & (((u >> sh) & 1) == 0)
        pair_t = ((u >> sh) == (r >> sh) + 1) & (((r >> sh) & 1) == 0)
        tside = ((rp >> sh) & 1) == 1
        lv.append((pair, pair_t, tside, jnp.where(tside, 1.0, -1.0).astype(F32)))
    bd = LEVELS[-1].bit_length() - 1
    diag = ((r >> bd) == (u >> bd)) & (u <= r)
    diag_t = ((r >> bd) == (u >> bd)) & (r <= u)
    return tri, tri_t, lv, diag, diag_t


def _row_of(pos, rev):
    return CHUNK - 1 - pos if rev else pos


def _chunk_terms(cum, b_scr, consts, rev):
    _, _, lv, _, _ = consts
    terms = []
    for b, (_, _, _, sgn) in zip(LEVELS, lv):
        pieces = []
        for j in range(CHUNK // (2 * b)):
            row = _row_of(2 * b * j + b - 1, rev)
            pieces.append(jnp.broadcast_to(b_scr[row:row + 1, :], (2 * b, DH)))
        if rev:
            pieces = pieces[::-1]
        bnd = pieces[0] if len(pieces) == 1 else jnp.concatenate(pieces, axis=0)
        terms.append(jnp.exp((cum - bnd) * sgn))
    b = LEVELS[-1]
    pieces = []
    for j in range(CHUNK // b):
        if j == 0:
            pieces.append(jnp.zeros((b, DH), F32))
        else:
            row = _row_of(b * j - 1, rev)
            pieces.append(jnp.broadcast_to(b_scr[row:row + 1, :], (b, DH)))
    if rev:
        pieces = pieces[::-1]
    start = jnp.concatenate(pieces, axis=0)
    wq = jnp.exp(jnp.minimum(cum - start, 0.0))
    wk = jnp.exp(jnp.minimum(start - cum, EXP_CLAMP))
    terms.append((wq, wk))
    return terms


def _run_staged(units):
    live = list(units)
    while live:
        nxt = []
        for u in live:
            try:
                next(u)
                nxt.append(u)
            except StopIteration:
                pass
        live = nxt


SCAN_TB = 256
SCAN_CB = SCAN_TB // CHUNK


def _block_order(i, ntb, rev):
    nctx = CTX // SCAN_TB
    if not rev:
        return i
    return jnp.where(i < nctx, nctx - 1 - i, ntb - 1 - (i - nctx))


def _chunk_in_block(j, rev):
    return SCAN_CB - 1 - j if rev else j


def _scan_fwd(q, k, v, g, rev):
    t = q.shape[0]
    nc = t // CHUNK
    hpb = SCAN_HEADS_FWD

    def body(q_ref, k_ref, v_ref, g_ref, o_ref, st_ref, s_scr, b_scr):
        consts = _scan_consts(rev)
        _, _, lv, diag, _ = consts
        masks = [lvl[0] for lvl in lv] + [diag]

        @pl.when(pl.program_id(1) == 0)
        def _():
            s_scr[...] = jnp.zeros_like(s_scr)

        tri = consts[0]
        state = {hh: s_scr[hh] for hh in range(hpb)}

        def unit(hh, j):
            sl = slice(hh * DH, (hh + 1) * DH)
            c = _chunk_in_block(j, rev)
            rows = slice(c * CHUNK, (c + 1) * CHUNK)
            b_ref = b_scr.at[hh * SCAN_CB + j]
            qc, kc, vc, gc = q_ref[rows, sl], k_ref[rows, sl], v_ref[rows, sl], g_ref[rows, sl]
            cum = _split_dot(tri, gc)
            b_ref[...] = cum
            yield
            terms = _chunk_terms(cum, b_ref, consts, rev)
            qf, kf = qc.astype(F32), kc.astype(F32)
            xs = [(jnp.where(tside, qf, kf) * w).astype(BF16) for w, (_, _, tside, _) in zip(terms[:-1], lv)]
            qd, kd = (qf * terms[-1][0]).astype(BF16), (kf * terms[-1][1]).astype(BF16)
            tot = _colsum(gc)
            qe = (qf * jnp.exp(cum)).astype(BF16)
            ke = (kf * jnp.exp(tot - cum)).astype(BF16)
            vb = vc.astype(BF16)
            yield
            scs = [_dot(x, x, NT) for x in xs] + [_dot(qd, kd, NT)]
            kv = _dot(vb, ke, TN)
            yield
            a = jnp.zeros((CHUNK, CHUNK), F32)
            for sc, m in zip(scs, masks):
                a = a + jnp.where(m, sc, 0.0)
            o_intra = _dot(a.astype(BF16), vb, NN)
            yield
            st = state[hh]
            st_ref[hh, c] = st
            o_ref[rows, sl] = o_intra + _dot(qe, st.astype(BF16), NT)
            state[hh] = st * jnp.exp(tot) + kv
            yield

        _run_staged([unit(hh, j) for hh in range(hpb) for j in range(SCAN_CB)])
        for hh in range(hpb):
            s_scr[hh] = state[hh]

    ntb = t // SCAN_TB
    col = pl.BlockSpec((SCAN_TB, hpb * DH), lambda h, i: (_block_order(i, ntb, rev), h))
    return pl.pallas_call(
        body, name="scan_fwd_" + ("bw" if rev else "fw"), grid=(NH // hpb, ntb),
        in_specs=[col] * 4,
        out_specs=[col, pl.BlockSpec((hpb, SCAN_CB, DH, DH), lambda h, i: (h, _block_order(i, ntb, rev), 0, 0))],
        out_shape=[jax.ShapeDtypeStruct((t, D), F32), jax.ShapeDtypeStruct((NH, nc, DH, DH), F32)],
        scratch_shapes=[pltpu.VMEM((hpb, DH, DH), F32), pltpu.VMEM((hpb * SCAN_CB, CHUNK, DH), F32)],
        compiler_params=_cp("parallel", "arbitrary"),
    )(q, k, v, g)


def _scan_bwd(q, k, v, g, do, states, rev):
    t = q.shape[0]
    nc = t // CHUNK
    hpb = SCAN_HEADS_BWD

    def body(q_ref, k_ref, v_ref, g_ref, do_ref, st_ref, dq_ref, dk_ref, dv_ref, dg_ref, ds_scr, b_scr):
        consts = _scan_consts(rev)
        _, tri_t, lv, diag, diag_t = consts
        masks = [(lvl[0], lvl[1]) for lvl in lv] + [(diag, diag_t)]
        @pl.when(pl.program_id(1) == 0)
        def _():
            ds_scr[...] = jnp.zeros_like(ds_scr)

        tri = consts[0]
        dstate = {hh: ds_scr[hh] for hh in range(hpb)}

        def unit(hh, jj):
            sl = slice(hh * DH, (hh + 1) * DH)
            c = _chunk_in_block(SCAN_CB - 1 - jj, rev)
            rows = slice(c * CHUNK, (c + 1) * CHUNK)
            b_ref = b_scr.at[hh * SCAN_CB + jj]
            qc, kc, vc, gc = q_ref[rows, sl], k_ref[rows, sl], v_ref[rows, sl], g_ref[rows, sl]
            dob = do_ref[rows, sl].astype(BF16)
            vb = vc.astype(BF16)
            cum = _split_dot(tri, gc)
            b_ref[...] = cum
            da = _dot(dob, vb, NT)
            da_t = _dot(vb, dob, NT)
            yield
            terms = _chunk_terms(cum, b_ref, consts, rev)
            qf, kf = qc.astype(F32), kc.astype(F32)
            xs = [(jnp.where(tside, qf, kf) * w).astype(BF16) for w, (_, _, tside, _) in zip(terms[:-1], lv)]
            wqd, wkd = terms[-1]
            qdb, kdb = (qf * wqd).astype(BF16), (kf * wkd).astype(BF16)
            tot = _colsum(gc)
            e_tot = jnp.exp(tot)
            e_b = jnp.exp(cum)
            e_t = jnp.exp(tot - cum)
            qeb = (qf * e_b).astype(BF16)
            keb = (kf * e_t).astype(BF16)
            dsym = [(jnp.where(m, da, 0.0) + jnp.where(m_t, da_t, 0.0)).astype(BF16) for m, m_t in masks[:-1]]
            dad = (jnp.where(diag, da, 0.0).astype(BF16), jnp.where(diag_t, da_t, 0.0).astype(BF16))
            yield
            sym = [_dot(x, x, NT) for x in xs]
            dxs = [_dot(d, x, NN) for d, x in zip(dsym, xs)]
            at_d = _dot(kdb, qdb, NT)
            dqt_d = _dot(dad[0], kdb, NN)
            dkt_d = _dot(dad[1], qdb, NN)
            qd = _dot(dob, qeb, TN)
            yield
            a_t = jnp.where(diag_t, at_d, 0.0)
            dq = dqt_d * wqd
            dk = dkt_d * wkd
            db = dqt_d * qdb.astype(F32) - dkt_d * kdb.astype(F32)
            for s, dx, x, w, (_, m_t, tside, sgn) in zip(sym, dxs, xs, terms[:-1], lv):
                a_t = a_t + jnp.where(m_t, s, 0.0)
                dxw = dx * w
                dq = dq + jnp.where(tside, dxw, 0.0)
                dk = dk + jnp.where(tside, 0.0, dxw)
                db = db + (dx * x.astype(F32)) * sgn
            dv_intra = _dot(a_t.astype(BF16), dob, NN)
            st = st_ref[hh, c]
            stb = st.astype(BF16)
            dqe = _dot(dob, stb, NN)
            yield
            dst = dstate[hh]
            dstb = dst.astype(BF16)
            dstate[hh] = dst * e_tot + qd
            dv_ref[rows, sl] = (dv_intra + _dot(keb, dstb, NT)).astype(BF16)
            dke = _dot(vb, dstb, NN)
            yield
            qe = qeb.astype(F32)
            ke = keb.astype(F32)
            dq_ref[rows, sl] = (dq + dqe * e_b).astype(BF16)
            dk_ref[rows, sl] = (dk + dke * e_t).astype(BF16)
            db = db + dqe * qe - dke * ke
            dtot = _colsum(dstb.astype(F32) * stb.astype(F32)) * e_tot + _colsum(dke * ke)
            dg_ref[rows, sl] = _split_dot(tri_t, db) + dtot
            yield

        _run_staged([unit(hh, jj) for hh in range(hpb) for jj in range(SCAN_CB)])
        for hh in range(hpb):
            ds_scr[hh] = dstate[hh]

    ntb = t // SCAN_TB
    blk = lambda i: _block_order(ntb - 1 - i, ntb, rev)
    col = pl.BlockSpec((SCAN_TB, hpb * DH), lambda h, i: (blk(i), h))
    out = jax.ShapeDtypeStruct((t, D), F32)
    outb = jax.ShapeDtypeStruct((t, D), BF16)
    return pl.pallas_call(
        body, name="scan_bwd_" + ("bw" if rev else "fw"), grid=(NH // hpb, ntb),
        in_specs=[col] * 5 + [pl.BlockSpec((hpb, SCAN_CB, DH, DH), lambda h, i: (h, blk(i), 0, 0))],
        out_specs=[col] * 4,
        out_shape=[outb] * 3 + [out],
        scratch_shapes=[pltpu.VMEM((hpb, DH, DH), F32), pltpu.VMEM((hpb * SCAN_CB, CHUNK, DH), F32)],
        compiler_params=_cp("parallel", "arbitrary"),
    )(q, k, v, g, do, states)


W_IN_GRAD_CHUNKS = (("a", (0, 512)), ("b", (0, 256)), ("b", (256, 512)))
W_IN_REF = 6688
W_IN_PAD = 896
W_IN_PIECE = 256
W_IN_STAGES = (3, 4)


def _assemble_w_in(g, rows, prev, name):
    n, r, wp = g.shape
    tr = W_IN_PIECE
    tiles = wp // DH
    first = rows[0] // tr

    def body(g_ref, *refs):
        o_ref = refs[-1]
        lane = lax.broadcasted_iota(jnp.int32, (tr, DH), 1)
        for t in range(W_IN_COLS // DH):
            acc = None
            for j in range(n):
                c = DH * t - W_IN_SHARD * j
                if c <= -DH or c >= W_IN_SHARD:
                    continue
                k, s = divmod(c, DH)
                lo = g_ref[j, :, k * DH:(k + 1) * DH] if 0 <= k < tiles else None
                hi = g_ref[j, :, (k + 1) * DH:(k + 2) * DH] if s and 0 <= k + 1 < tiles else None
                if s:
                    zero = jnp.zeros((tr, DH), g.dtype)
                    lo = zero if lo is None else pltpu.roll(lo, DH - s, 1)
                    hi = zero if hi is None else pltpu.roll(hi, DH - s, 1)
                    part = jnp.where(lane < DH - s, lo, hi)
                else:
                    part = lo
                acc = part if acc is None else acc + part
            o_ref[:, t * DH:(t + 1) * DH] = jnp.zeros((tr, DH), g.dtype) if acc is None else acc

    held = [] if prev is None else [prev]
    return pl.pallas_call(
        body, name=name, grid=((rows[1] - rows[0]) // tr,),
        in_specs=[pl.BlockSpec((n, tr, wp), lambda i: (0, first + i, 0))] + [pl.BlockSpec(memory_space=pl.ANY)] * len(held),
        out_specs=pl.BlockSpec((tr, W_IN_COLS), lambda i: (first + i, 0)),
        out_shape=jax.ShapeDtypeStruct((r, W_IN_COLS), g.dtype),
        input_output_aliases={1: 0} if held else {},
        compiler_params=_cp("parallel"),
    )(g, *held)


def _gate_cols(w):
    return jnp.pad(w, ((0, 0), (GOFF, GW - GOFF - D)))


def _gate_rows(w):
    return jnp.pad(w, ((GOFF, GW - GOFF - D), (0, 0)))


def _layout_wgk(w):
    r = w.shape[1]
    top = jnp.concatenate([w[0], jnp.zeros_like(w[0])], axis=1)
    bot = jnp.concatenate([jnp.zeros_like(w[1]), w[1]], axis=1)
    return jnp.concatenate([top, bot, jnp.zeros((DH - 2 * r, D), w.dtype)], axis=0)


def _unlayout_wgk(d, r=16):
    return jnp.stack([d[:r, :HW], d[r:2 * r, HW:]])


def _local_step(z, target, modc, modx, norms, onw, hg_lb, wgk, bgk, get_w_in, get_mix, get_ffn, send):
    n_pre1, n_post1, n_pre2, n_post2 = norms
    t = z[0].shape[0] + z[1].shape[0]
    tm = 1152 if t % 1152 == 0 else 256
    h1 = _prenorm(z, n_pre1, modc, modx, 0, 1, "prenorm1")
    w_in = get_w_in(h1)
    p = _matmul(h1, w_in, NN, BF16, "mm_in", t, 1024, D)
    q, v, k_f, k_b, g_f, g_b = _gates_fwd(p, hg_lb, wgk, bgk)
    o_f, st_f = _scan_fwd(q, k_f, v, g_f, False)
    o_b, st_b = _scan_fwd(q, k_b, v, g_b, True)
    y = _post_fwd(o_f, o_b, p, onw)
    w_br_hg, w_br_gla, w_out = get_mix(y)
    u1, u2, merged = _branch_merge(y, w_br_hg, w_br_gla, p)
    y1 = _matmul(merged, w_out, NN, BF16, "mm_out", tm, 512, GW)
    z1, h2 = _mid_fwd(z, y1, n_post1, n_pre2, modc, modx)
    w_gu_t, w_down = get_ffn(h2)
    u, v_ff, act = _mm_gu_act(h2, w_gu_t[0], w_gu_t[1], "mm_gu", tm)
    y2 = _matmul(act, w_down, NN, BF16, "mm_down", t, 512, D_FF)
    dz, dy2, loss_vec, sm_final = _final(z1, y2, target, n_post2, modc, modx)
    du, dv_ff = _mm_down_dx_act(dy2, w_down, u, v_ff, "mm_down_dx", tm)
    d_w_down = _matmul(act, dy2, TN, BF16, "mm_down_dw", D_FF // 2, 1024, t)
    dh2 = _matmul((du, dv_ff), w_gu_t, NN, BF16, "mm_gu_dx", tm, 512, D_FF)
    d_w_gate_t = _matmul(du, h2, TN, BF16, "mm_gate_dw", D_FF // 2, 1024, t)
    d_w_up_t = _matmul(dv_ff, h2, TN, BF16, "mm_up_dw", D_FF // 2, 1024, t)
    dh2 = send(("w_down", "w_gate_t", "w_up_t"), (d_w_down, d_w_gate_t, d_w_up_t), dh2)
    dz, dy1, sm_mid = _mid_bwd(dh2, dz, z1, y1, n_post1, n_pre2, modc, modx)
    dmerged = _matmul(dy1, w_out, NT, BF16, "mm_out_dx", tm, GW, D)
    d_w_out = _matmul(merged, dy1, TN, BF16, "mm_out_dw", GW, 512, t)
    du1, du2, dgm, dy_hg, dy_gla = _branch_merge_bwd(dmerged, p, u1, u2, w_br_hg, w_br_gla)
    d_w_br_hg = _matmul(y, du1, TN, BF16, "mm_br_hg_dw", HW, GW, t, a_off=0, m_out=HW)
    d_w_br_gla = _matmul(y, du2, TN, BF16, "mm_br_gla_dw", HW, GW, t, a_off=1, m_out=HW)
    dy_hg = send(("w_out", "w_br_hg", "w_br_gla"), (d_w_out, d_w_br_hg, d_w_br_gla), dy_hg)
    do, dgo, sm_post = _post_bwd(dy_hg, dy_gla, o_f, o_b, p, onw)
    dq_f, dk_f, dv_f, dg_f = _scan_bwd(q, k_f, v, g_f, do, st_f, False)
    dq_b, dk_b, dv_b, dg_b = _scan_bwd(q, k_b, v, g_b, do, st_b, True)
    dp, d_lb, d_wgk, d_bgk = _gates_bwd(p, hg_lb, wgk, bgk, dgm, dgo, dq_f, dq_b, dv_f, dv_b, dk_f, dk_b, dg_f, dg_b)
    d_w_in_a = _matmul(h1, dp, TN, BF16, "mm_in_dw_a", 512, 1024, t, a_off=0, m_out=D // 2)
    dp = send(("w_in_a",), (d_w_in_a,), dp)
    d_w_in_b = _matmul(h1, dp, TN, BF16, "mm_in_dw_b", 512, 1024, t, a_off=1, m_out=D // 2)
    dp = send(("w_in_b",), (d_w_in_b,), dp)
    dh1 = _matmul(dp, w_in, NT, BF16, "mm_in_dx", tm, 512, W_IN_COLS // 2)
    grad_x, sm_pre = _pre_bwd(dh1, dz, z, n_pre1, modc, modx)
    return dict(loss_vec=loss_vec, grad_x=grad_x, sm_final=sm_final, sm_mid=sm_mid, sm_post=sm_post, sm_pre=sm_pre,
                d_lb=d_lb, d_wgk=d_wgk, d_bgk=d_bgk)


MESH = pl.DeviceIdType.MESH
ANY = pl.BlockSpec(memory_space=pl.ANY)
N_REL = N_DEV - 1


def _place():
    return lax.axis_index("x"), lax.axis_index("y"), lax.axis_index("c")


def _slot(p):
    return 4 * p[0] + 2 * p[1] + p[2]


HBM = pl.BlockSpec(memory_space=pltpu.HBM)
SEM = pl.BlockSpec(memory_space=pltpu.SEMAPHORE)
EFFECT = pltpu.SideEffectType.DATAFLOW_SIDE_EFFECTING


def _peer_of(x, y, c, k):
    flip = lambda v, bit: 1 - v if bit else v
    return flip(x, k & 4), flip(y, k & 2), flip(c, k & 1)


def _view_whole(src, slot):
    return src


def _view_near(src, slot):
    return src


_view_near.peers = (1, 2, 4, 6)


def _view_near_rows(rows):
    def view(src, slot):
        return src.at[pl.ds(rows[0], rows[1] - rows[0])]
    view.peers = _view_near.peers
    view.land = lambda land, slot: land.at[slot, pl.ds(rows[0], rows[1] - rows[0])]
    return view


def _view_block(src, slot):
    return src.at[slot]


W_IN_SHARD = W_IN_REF // N_DEV


def _view_window(rows):
    def view(src, slot):
        col0 = pl.multiple_of((W_IN_SHARD * slot // DH) * DH, DH)
        return src.at[pl.ds(rows[0], rows[1] - rows[0]), pl.ds(col0, D)]
    return view


def _split_copies(view, srcs, lands, send_sems, recv_sems, local_sems):
    x, y, c = _place()
    me = _slot((x, y, c))
    into = getattr(view, "land", lambda land, slot: land.at[slot])
    local, sends, waits = [], [], []
    for a, (src, land) in enumerate(zip(srcs, lands)):
        local.append(pltpu.make_async_copy(view(src, me), into(land, me), local_sems.at[a]))
        for k in getattr(view, "peers", range(1, N_DEV)):
            peer = _peer_of(x, y, c, k)
            mine = view(src, _slot(peer))
            sems = dict(send_sem=send_sems.at[N_REL * a + k - 1], recv_sem=recv_sems.at[N_REL * a + k - 1],
                        device_id=peer, device_id_type=MESH)
            sends.append(pltpu.make_async_remote_copy(src_ref=mine, dst_ref=into(land, me), **sems))
            waits.append(pltpu.make_async_remote_copy(src_ref=mine, dst_ref=into(land, _slot(peer)), **sems))
    return local, sends, waits


def _split_start(groups, name, after):
    built = []
    for view, srcs, lands in groups:
        lands = [lax.empty(l, s.dtype) if isinstance(l, tuple) else l for l, s in zip(lands, srcs)]
        built.append((view, list(srcs), lands))
    bufs = [b for _, srcs, lands in built for b in srcs + lands]
    nb, ng = len(bufs), len(built)

    def body(*refs):
        buf_refs, sem_refs, token = refs[:nb], refs[nb + 1:nb + 1 + 3 * ng], refs[-1]
        pos = 0
        for i, (view, srcs, _) in enumerate(built):
            n = len(srcs)
            local, sends, _ = _split_copies(view, buf_refs[pos:pos + n], buf_refs[pos + n:pos + 2 * n],
                                            *sem_refs[3 * i:3 * i + 3])
            pos += 2 * n
            for cp in local + sends:
                cp.start()
        token[...] = jnp.zeros_like(token)

    sems = []
    for _, srcs, _ in built:
        n = len(srcs)
        sems += [pltpu.SemaphoreType.DMA((N_REL * n,)), pltpu.SemaphoreType.DMA((N_REL * n,)),
                 pltpu.SemaphoreType.DMA((n,))]
    hbm = lambda a: pltpu.with_memory_space_constraint(a, pltpu.HBM)
    out = pl.pallas_call(
        body, name=name,
        out_shape=(*sems, *[pltpu.HBM(b.shape, b.dtype) for b in bufs], jax.ShapeDtypeStruct((8, DH), F32)),
        in_specs=[HBM] * nb + [ANY],
        out_specs=(*([SEM] * (3 * ng)), *([HBM] * nb), pl.BlockSpec(memory_space=pltpu.VMEM)),
        input_output_aliases={i: 3 * ng + i for i in range(nb)},
        compiler_params=pltpu.CompilerParams(has_side_effects=EFFECT),
    )(*[hbm(b) for b in bufs], after)
    handles, pos = [], 3 * ng
    for i, (view, srcs, _) in enumerate(built):
        n = len(srcs)
        handles.append(dict(view=view, n=n, sems=out[3 * i:3 * i + 3], srcs=list(out[pos:pos + n]),
                            lands=list(out[pos + n:pos + 2 * n])))
        pos += 2 * n
    return handles, out[-1]


def _split_wait(handle, name, after, srcs=None, lands=None):
    view, n, sems = handle["view"], handle["n"], handle["sems"]
    srcs = handle["srcs"] if srcs is None else srcs
    lands = handle["lands"] if lands is None else lands
    afters = list(after) if isinstance(after, (list, tuple)) else [after]

    def body(*refs):
        src_refs, land_refs = refs[:n], refs[n:2 * n]
        send_sems, recv_sems, local_sems = refs[2 * n:2 * n + 3]
        local, _, waits = _split_copies(view, src_refs, land_refs, send_sems, recv_sems, local_sems)
        for cp in waits:
            cp.wait_send()
            cp.wait_recv()
        for cp in local:
            cp.wait()

    out = pl.pallas_call(
        body, name=name,
        out_shape=(*[pltpu.HBM(s.shape, s.dtype) for s in srcs], *[pltpu.HBM(l.shape, l.dtype) for l in lands]),
        in_specs=[HBM] * (2 * n) + [SEM, SEM, SEM] + [ANY] * len(afters),
        out_specs=tuple([HBM] * (2 * n)),
        input_output_aliases={i: i for i in range(2 * n)},
        compiler_params=pltpu.CompilerParams(has_side_effects=EFFECT),
    )(*srcs, *lands, *sems, *afters)
    handle["srcs"] = list(out[:n])
    return list(out[n:])


def _tie(x, token, name):
    def body(x_ref, t_ref, o_ref):
        pass

    return pl.pallas_call(
        body, name=name, out_shape=jax.ShapeDtypeStruct(x.shape, x.dtype),
        in_specs=[ANY, ANY], out_specs=ANY, input_output_aliases={0: 0},
    )(x, token)


def _forward_to_sibling(land, name, rows):
    def body(land_ref, out_ref, send_sems, recv_sems):
        x, y, c = _place()
        sibling = (x, y, 1 - c)
        chips = [(1 - x, y), (x, 1 - y), (1 - x, 1 - y)]
        piece = pl.ds(rows[0], rows[1] - rows[0])

        def copy(j, core):
            blk = _slot((*chips[j], core))
            return pltpu.make_async_remote_copy(src_ref=land_ref.at[blk, piece], dst_ref=out_ref.at[blk, piece],
                                                send_sem=send_sems.at[j], recv_sem=recv_sems.at[j],
                                                device_id=sibling, device_id_type=MESH)

        sends = [copy(j, c) for j in range(3)]
        for cp in sends:
            cp.start()
        for j in range(3):
            copy(j, 1 - c).wait_recv()
        for cp in sends:
            cp.wait_send()

    return pl.pallas_call(
        body, name=name, in_specs=[ANY], out_specs=ANY, input_output_aliases={0: 0},
        out_shape=jax.ShapeDtypeStruct(land.shape, land.dtype),
        scratch_shapes=[pltpu.SemaphoreType.DMA((3,)), pltpu.SemaphoreType.DMA((3,))],
    )(land)


def _mod_fwd(a, w, b):
    def body(a_ref, w_ref, b_ref, o_ref):
        o_ref[...] = _dot(_silu(a_ref[...]), w_ref[...], NN, precision=HI) + b_ref[...]

    return pl.pallas_call(
        body, name="mod_fwd", out_shape=jax.ShapeDtypeStruct((a.shape[0], w.shape[1]), F32),
        compiler_params=pltpu.CompilerParams(vmem_limit_bytes=VMEM_LIMIT),
    )(a, w, b)


def _mod_bwd(a, d, w):
    def body(a_ref, d_ref, w_ref, dw_ref, dc_ref):
        av = a_ref[...]
        dv = d_ref[...]
        dw_ref[...] = _dot(_silu(av), dv, TN, precision=HI)
        da = _dot(dv[0:8, :], w_ref[...], NT, precision=HI) * _dsilu(av[0:8, :])
        row = lax.broadcasted_iota(jnp.int32, da.shape, 0)
        dc_ref[...] = jnp.where(row == 0, da, 0.0)

    return pl.pallas_call(
        body, name="mod_bwd",
        out_shape=[jax.ShapeDtypeStruct(w.shape, F32), jax.ShapeDtypeStruct((8, w.shape[0]), F32)],
        compiler_params=pltpu.CompilerParams(vmem_limit_bytes=VMEM_LIMIT),
    )(a, d, w)


def _sum_devices(g):
    def body(g_ref, o_ref):
        acc = g_ref[0]
        for i in range(1, g.shape[0]):
            acc = acc + g_ref[i]
        o_ref[...] = acc

    return pl.pallas_call(body, name="sum_devices_%d" % g.shape[1],
                          out_shape=jax.ShapeDtypeStruct(g.shape[1:], F32))(g)


def _sum_windows(g, name, col0, prev):
    n, r, c = g.shape
    tr = 128
    first = col0 // tr

    def body(g_ref, *refs):
        o_ref = refs[-1]
        x, y, cc = _place()
        lane0 = (W_IN_SHARD * _slot((x, y, cc))) % DH
        acc = g_ref[0].astype(F32)
        for i in range(1, n):
            acc = acc + g_ref[i].astype(F32)
        o_ref[...] = pltpu.roll(acc, (c - lane0) % c, 1).T

    held = [] if prev is None else [prev]
    return pl.pallas_call(
        body, name=name, grid=(r // tr,),
        in_specs=[pl.BlockSpec((n, tr, c), lambda i: (0, i, 0))] + [pl.BlockSpec(memory_space=pl.ANY)] * len(held),
        out_specs=pl.BlockSpec((c, tr), lambda i: (0, first + i)),
        out_shape=jax.ShapeDtypeStruct((c, D), F32),
        input_output_aliases={1: 0} if held else {},
        compiler_params=_cp("parallel"),
    )(g, *held)


def _adam_rows(r, c, n):
    budget = 10 * 1024 * 1024
    best = None
    for tr in range(16, r + 1, 16):
        if r % tr == 0 and tr * c * (2 * n + 28) <= budget:
            best = tr
    return best if best is not None else r


def _adamw(g, w, m, v, name):
    n, r, c = g.shape
    tr = _adam_rows(r, c, n)
    bc1 = 1.0 - ADAM_B1 ** ADAM_STEP
    bc2 = 1.0 - ADAM_B2 ** ADAM_STEP

    def body(g_ref, w_ref, m_ref, v_ref, go_ref, d_ref, mo_ref, vo_ref):
        grad = g_ref[0].astype(F32)
        for i in range(1, n):
            grad = grad + g_ref[i].astype(F32)
        go_ref[...] = grad
        m_new = ADAM_B1 * m_ref[...] + (1.0 - ADAM_B1) * grad
        v_new = ADAM_B2 * v_ref[...] + (1.0 - ADAM_B2) * (grad * grad)
        mo_ref[...] = m_new
        vo_ref[...] = v_new
        d_ref[...] = -ADAM_LR * ((m_new / bc1) / (jnp.sqrt(v_new / bc2) + ADAM_EPS) + ADAM_WD * w_ref[...])

    blk = pl.BlockSpec((tr, c), lambda i: (i, 0))
    out = jax.ShapeDtypeStruct((r, c), F32)
    return pl.pallas_call(
        body, name=name, grid=(r // tr,),
        in_specs=[pl.BlockSpec((n, tr, c), lambda i: (0, i, 0)), blk, blk, blk],
        out_specs=[blk] * 4, out_shape=[out] * 4,
        compiler_params=_cp("parallel"),
    )(g, w, m, v)


ADAM_ROWS3 = 168


def _adam_math(grad, w, m, v):
    bc1 = 1.0 - ADAM_B1 ** ADAM_STEP
    bc2 = 1.0 - ADAM_B2 ** ADAM_STEP
    m_new = ADAM_B1 * m + (1.0 - ADAM_B1) * grad
    v_new = ADAM_B2 * v + (1.0 - ADAM_B2) * (grad * grad)
    delta = -ADAM_LR * ((m_new / bc1) / (jnp.sqrt(v_new / bc2) + ADAM_EPS) + ADAM_WD * w)
    return delta, m_new, v_new


def _adamw_rows3(g, w3, m3, v3, name):
    r, _, c = w3.shape
    n = ADAM_ROWS3
    starts = list(range(0, r - n, n)) + [r - n]

    def body(g_hbm, w_hbm, m_hbm, v_hbm, go_hbm, d_hbm, mo_hbm, vo_hbm, gbuf, ibuf, obuf, in_sems, out_sems):
        def fetch(p):
            r0, slot = starts[p], p % 2
            g0 = (r0 // 8) * 8
            cps = [pltpu.make_async_copy(g_hbm.at[pl.ds(g0, n + 8)], gbuf.at[slot], in_sems.at[slot, 0])]
            cps += [pltpu.make_async_copy(h.at[pl.ds(r0, n), 0], ibuf.at[slot, k], in_sems.at[slot, 1 + k])
                    for k, h in enumerate((w_hbm, m_hbm, v_hbm))]
            for cp in cps:
                cp.start()
            return cps

        pending, outs = fetch(0), []
        for p, r0 in enumerate(starts):
            slot = p % 2
            nxt = fetch(p + 1) if p + 1 < len(starts) else []
            for cp in pending:
                cp.wait()
            grad = gbuf[slot, pl.ds(r0 - (r0 // 8) * 8, n), :]
            delta, m_new, v_new = _adam_math(grad, ibuf[slot, 0], ibuf[slot, 1], ibuf[slot, 2])
            for cp in outs:
                cp.wait()
            for k, val in enumerate((grad, delta, m_new, v_new)):
                obuf[slot, k] = val
            outs = [pltpu.make_async_copy(obuf.at[slot, k], h.at[pl.ds(r0, n), 0], out_sems.at[slot, k])
                    for k, h in enumerate((go_hbm, d_hbm, mo_hbm, vo_hbm))]
            for cp in outs:
                cp.start()
            pending = nxt
        for cp in outs:
            cp.wait()

    out = jax.ShapeDtypeStruct(w3.shape, F32)
    return pl.pallas_call(
        body, name=name, in_specs=[ANY] * 4, out_specs=[ANY] * 4, out_shape=[out] * 4,
        scratch_shapes=[pltpu.VMEM((2, n + 8, c), F32), pltpu.VMEM((2, 3, n, c), F32), pltpu.VMEM((2, 4, n, c), F32),
                        pltpu.SemaphoreType.DMA((2, 4)), pltpu.SemaphoreType.DMA((2, 4))],
        compiler_params=pltpu.CompilerParams(vmem_limit_bytes=VMEM_LIMIT),
    )(g, w3, m3, v3)


def kernel(x, c, ctx, c_ctx, w_mod, b_mod, norm_pre1, norm_post1, norm_pre2, norm_post2, w_in, hg_lb, hg_onorm, gla_w_gk, gla_b_gk, gla_onorm, w_br_hg, w_br_gla, w_out, w_ff_gate, w_ff_up, w_ff_down, loss_target, m_c_ctx, m_w_mod, m_b_mod, m_norm_pre1, m_norm_post1, m_norm_pre2, m_norm_post2, m_w_in, m_hg_lb, m_hg_onorm, m_gla_w_gk, m_gla_b_gk, m_gla_onorm, m_w_br_hg, m_w_br_gla, m_w_out, m_w_ff_gate, m_w_ff_up, m_w_ff_down, v_c_ctx, v_w_mod, v_b_mod, v_norm_pre1, v_norm_post1, v_norm_pre2, v_norm_post2, v_w_in, v_hg_lb, v_hg_onorm, v_gla_w_gk, v_gla_b_gk, v_gla_onorm, v_w_br_hg, v_w_br_gla, v_w_out, v_w_ff_gate, v_w_ff_up, v_w_ff_down):
    xi, yi, ci = lax.axis_index("x"), lax.axis_index("y"), lax.axis_index("c")
    me = 4 * xi + 2 * yi + ci
    t = CTX + x.shape[1]

    w_in_pieces, w_in_state = [], {}

    def w_in_piece(i):
        return (_view_near_rows((i * W_IN_PIECE, (i + 1) * W_IN_PIECE)), w_in_state["src"], w_in_state["land"])

    def started_w_in(handle):
        w_in_state.update(src=handle["srcs"], land=handle["lands"])
        w_in_pieces.append(handle)

    tr_ = lambda a: jnp.swapaxes(a[0], 0, 1)
    w_in_bf = jnp.pad(w_in[0].astype(BF16), ((0, 0), (0, W_IN_PAD - W_IN_SHARD)))
    w_in_state.update(src=[w_in_bf], land=[lax.empty((N_DEV,) + w_in_bf.shape, BF16)])
    gathered = lambda arrs: [(N_DEV,) + a.shape for a in arrs]
    whole = lambda arrs: (_view_whole, arrs, gathered(arrs))
    small_in = [c, hg_lb, gla_w_gk[0], gla_b_gk[0]]
    (small_handle, piece), tok = _split_start([whole(small_in), w_in_piece(0)], "ag_small_start", c)
    started_w_in(piece)
    c_all, lb_g, wgk_g, bgk_g = _split_wait(small_handle, "ag_small_wait", tok)
    big = [w_in[0], w_br_hg[0], w_br_gla[0], w_out[0], tr_(w_ff_gate), tr_(w_ff_up), w_ff_down[0]]
    big_bf = [None] + [w.astype(BF16) for w in big[1:]]
    cols = lambda g: jnp.transpose(g, (1, 0, 2)).reshape(g.shape[1], N_DEV * g.shape[2])

    def get_w_in(after):
        w_full, first = None, 0
        for s, last in enumerate(W_IN_STAGES):
            for i in range(first, last):
                land = _split_wait(w_in_pieces[i], "ag_w_in_wait%d" % i, after if w_full is None else [after, w_full],
                                   srcs=w_in_state["src"], lands=w_in_state["land"])
                w_in_state.update(src=w_in_pieces[i]["srcs"], land=land)
            rows = (first * W_IN_PIECE, last * W_IN_PIECE)
            w_in_state["land"] = [_forward_to_sibling(w_in_state["land"][0], "ag_w_in_forward%d" % s, rows)]
            w_full = _assemble_w_in(w_in_state["land"][0], rows, w_full, "assemble_w_in%d" % s)
            first = last
        return w_full

    def get_mix(after):
        g_brh, g_brg, g_out = _split_wait(mix_handle, "ag_mix_wait", after)
        return _gate_cols(cols(g_brh)), _gate_cols(cols(g_brg)), _gate_rows(g_out.reshape(D, D))

    def get_ffn(after):
        g_gate, g_up, g_down = _split_wait(ffn_handle, "ag_ffn_wait", after)
        return (g_gate.reshape(D_FF, D), g_up.reshape(D_FF, D)), g_down.reshape(D_FF, D)

    hg_lb_full = jnp.transpose(lb_g, (1, 2, 0, 3)).reshape(2, 2, HW)
    wgk_k = _layout_wgk(jnp.transpose(wgk_g, (1, 2, 0, 3)).reshape(2, 16, HW)).astype(BF16)
    bgk_k = jnp.transpose(bgk_g, (1, 0, 2)).reshape(1, D)
    onw = jnp.concatenate([jnp.tile(hg_onorm, (1, NH // 2)), jnp.tile(gla_onorm, (1, NH // 2))], axis=1)

    n_mod = w_mod.shape[2]
    a9 = jnp.concatenate([c_ctx[None], c_all[:, 0], jnp.zeros((16 - 1 - N_DEV, D), F32)], axis=0)
    b_loc = lax.dynamic_slice(b_mod, (0, me * n_mod), (1, n_mod))
    s_loc = _mod_fwd(a9, w_mod[0], b_loc)
    (mod_handle, piece), tok = _split_start([whole([s_loc]), w_in_piece(1)], "ag_mod_start", s_loc)
    started_w_in(piece)
    for i in range(2, D // W_IN_PIECE):
        (piece,), tok = _split_start([w_in_piece(i)], "ag_w_in_start%d" % i, tok)
        started_w_in(piece)
    s_all, = _split_wait(mod_handle, "ag_mod_wait", tok)
    mod_all = jnp.transpose(s_all, (1, 0, 2)).reshape(16, N_DEV * n_mod)
    pad8 = lambda m: jnp.concatenate([m.reshape(6, D), jnp.zeros((2, D), F32)], axis=0)
    modc = pad8(mod_all[0])
    modx = pad8(lax.dynamic_slice(mod_all, (1 + me, 0), (1, N_DEV * n_mod))[0])

    (mix_handle, ffn_handle), tok = _split_start([whole(big_bf[1:4]), whole(big_bf[4:])], "ag_big_start", s_all)

    z = (ctx[0], x[0])
    modx = _tie(modx, tok, "tie_mod")
    norms = (norm_pre1, norm_post1, norm_pre2, norm_post2)
    shard = lambda d: jnp.transpose(d.reshape(d.shape[0], N_DEV, -1), (1, 0, 2)).astype(BF16)
    rowshard = lambda d: d.reshape(N_DEV, d.shape[0] // N_DEV, d.shape[1]).astype(BF16)
    sent, w_in_grad = [], {}

    def w_in_chunk(i):
        half, rows = W_IN_GRAD_CHUNKS[i]
        return (_view_window(rows), w_in_grad[half], [(N_DEV, rows[1] - rows[0], D)])

    def sent_w_in(i, handle):
        w_in_grad[W_IN_GRAD_CHUNKS[i][0]] = handle["srcs"]
        sent.append(("w_in%d" % i, ["w_in#%d" % i], handle))

    def send(names, grads, x_after):
        if names == ("w_in_a",):
            w_in_grad["a"] = list(grads)
            (handle,), tok = _split_start([w_in_chunk(0)], "grads_w_in0_start", x_after)
            sent_w_in(0, handle)
            return _tie(x_after, tok, "tie_w_in0")
        if names == ("w_in_b",):
            w_in_grad["b"] = list(grads)
            return x_after
        arrs, leaves = [], []
        for nm, g in zip(names, grads):
            if nm in ("w_gate_t", "w_up_t"):
                arrs.append(rowshard(g))
                leaves.append({"w_gate_t": "w_ff_gate", "w_up_t": "w_ff_up"}[nm])
            elif nm == "w_down":
                arrs.append(rowshard(g))
                leaves.append("w_ff_down")
            elif nm == "w_out":
                arrs.append(rowshard(g[GOFF:GOFF + D]))
                leaves.append(nm)
            else:
                arrs.append(shard(g[:, GOFF:GOFF + D]))
                leaves.append(nm)
        (handle,), tok = _split_start([(_view_block, arrs, [a.shape for a in arrs])], "grads_%s_start" % names[0],
                                      x_after)
        sent.append((names[0], leaves, handle))
        return _tie(x_after, tok, "tie_" + names[0])

    r = _local_step(z, loss_target[0], modc, modx, norms, onw, hg_lb_full, wgk_k, bgk_k,
                    get_w_in, get_mix, get_ffn, send)
    grad_x = r["grad_x"][None]

    sm_pre, sm_mid, sm_fin = r["sm_pre"], r["sm_mid"], r["sm_final"]
    dmodc = jnp.stack([sm_pre[0], sm_pre[2], sm_mid[4], sm_mid[0], sm_mid[2], sm_fin[0]]).reshape(-1)
    dmodx = jnp.stack([sm_pre[1], sm_pre[3], sm_mid[5], sm_mid[1], sm_mid[3], sm_fin[1]]).reshape(-1)
    on = r["sm_post"][0].reshape(NH, DH)
    pieces = [dmodc, dmodx, sm_pre[4], sm_mid[7], sm_mid[6], sm_fin[2], on[:NH // 2].sum(0), on[NH // 2:].sum(0),
              r["d_lb"][:2].reshape(-1), _unlayout_wgk(r["d_wgk"]).reshape(-1), r["d_bgk"][0]]
    loss_local = (0.5 / D) * jnp.sum(r["loss_vec"])
    pieces.append(jnp.concatenate([loss_local.reshape(1), jnp.zeros((DH - 1,), F32)]))
    sizes = [p.shape[0] for p in pieces]
    pack = jnp.concatenate(pieces).reshape(-1, DH)
    moms = [(m_w_in, v_w_in), (m_w_br_hg, v_w_br_hg), (m_w_br_gla, v_w_br_gla), (m_w_out, v_w_out),
            (m_w_ff_gate, v_w_ff_gate), (m_w_ff_up, v_w_ff_up), (m_w_ff_down, v_w_ff_down)]
    names = ["w_in", "w_br_hg", "w_br_gla", "w_out", "w_ff_gate", "w_ff_up", "w_ff_down"]
    wmv = {nm: (w, m, v) for nm, w, (m, v) in zip(names, big, moms)}
    res = {}

    def update(nm):
        w, m, v = wmv[nm]
        if nm in ("w_ff_gate", "w_ff_up"):
            outs = _adamw(recv[nm], w, tr_(m), tr_(v), "adamw_" + nm)
            res[nm] = [jnp.swapaxes(o, 0, 1)[None] for o in outs]
        else:
            res[nm] = [o[None] for o in _adamw(recv[nm], w, m[0], v[0], "adamw_" + nm)]

    (small_handle, handle), tok = _split_start([whole([pack]), w_in_chunk(1)], "small_grads_start", pack)
    sent_w_in(1, handle)
    recv = {}
    for first, leaves, handle in sent:
        if not first.startswith("w_in"):
            recv.update(zip(leaves, _split_wait(handle, "grads_%s_wait" % first, tok)))
    update("w_ff_gate")
    update("w_ff_up")
    pack_all, = _split_wait(small_handle, "small_grads_wait", [res["w_ff_gate"][0], res["w_ff_up"][0]])
    tot = _sum_devices(pack_all).reshape(-1)
    offs = [sum(sizes[:i]) for i in range(len(sizes))]
    part = lambda i: tot[offs[i]:offs[i] + sizes[i]]
    dmodc_t, dmodx_t = part(0), part(1)
    g_b_mod = (dmodc_t + dmodx_t)[None]
    g_norms = [part(i)[None] for i in (2, 3, 4, 5)]
    g_hg_on, g_gla_on = part(6)[None], part(7)[None]
    lb0 = lax.dynamic_slice(part(8).reshape(2, HW), (0, me * (HW // N_DEV)), (2, HW // N_DEV))
    g_hg_lb = jnp.stack([lb0, -lb0])
    g_wgk = lax.dynamic_slice(part(9).reshape(2, 16, HW), (0, 0, me * (HW // N_DEV)), (2, 16, HW // N_DEV))[None]
    g_bgk = lax.dynamic_slice(part(10).reshape(2, HW), (0, me * (HW // N_DEV)), (2, HW // N_DEV))[None]
    loss = part(11)[0]

    dmx_all = pack_all.reshape(N_DEV, -1)[:, sizes[0]:sizes[0] + sizes[1]]
    d9 = jnp.concatenate([lax.dynamic_slice(dmodc_t[None], (0, me * n_mod), (1, n_mod)),
                          lax.dynamic_slice(dmx_all, (0, me * n_mod), (N_DEV, n_mod)),
                          jnp.zeros((16 - 1 - N_DEV, n_mod), F32)], axis=0)
    g_w_mod, dcc_part = _mod_bwd(a9, d9, w_mod[0])
    (cctx_handle, handle), tok = _split_start([whole([dcc_part]), w_in_chunk(2)], "c_ctx_start", dcc_part)
    sent_w_in(2, handle)
    recv["w_ff_down"] = _tie(recv["w_ff_down"], tok, "tie_down")
    update("w_ff_down")
    res["w_mod"] = [o[None] for o in _adamw(g_w_mod[None], w_mod[0], m_w_mod[0], v_w_mod[0], "adamw_w_mod")]
    for nm in ("w_out", "w_br_hg", "w_br_gla"):
        update(nm)
    dcc_all, = _split_wait(cctx_handle, "c_ctx_wait", [res["w_ff_down"][0], res["w_mod"][0]])
    g_c_ctx = _sum_devices(dcc_all)[0]

    small = [("c_ctx", c_ctx, m_c_ctx, v_c_ctx, g_c_ctx), ("b_mod", b_mod, m_b_mod, v_b_mod, g_b_mod),
             ("norm_pre1", norm_pre1, m_norm_pre1, v_norm_pre1, g_norms[0]),
             ("norm_post1", norm_post1, m_norm_post1, v_norm_post1, g_norms[1]),
             ("norm_pre2", norm_pre2, m_norm_pre2, v_norm_pre2, g_norms[2]),
             ("norm_post2", norm_post2, m_norm_post2, v_norm_post2, g_norms[3]),
             ("hg_lb", hg_lb, m_hg_lb, v_hg_lb, g_hg_lb), ("hg_onorm", hg_onorm, m_hg_onorm, v_hg_onorm, g_hg_on),
             ("gla_w_gk", gla_w_gk, m_gla_w_gk, v_gla_w_gk, g_wgk), ("gla_b_gk", gla_b_gk, m_gla_b_gk, v_gla_b_gk, g_bgk),
             ("gla_onorm", gla_onorm, m_gla_onorm, v_gla_onorm, g_gla_on)]
    flat = lambda k: jnp.concatenate([s[k].reshape(-1) for s in small]).reshape(-1, DH)
    outs = _adamw(flat(4)[None], flat(1), flat(2), flat(3), "adamw_small")
    off = 0
    for nm, w, _, _, _ in small:
        res[nm] = [o.reshape(-1)[off:off + w.size].reshape(w.shape) for o in outs]
        off += w.size

    done = [res[nm][0] for nm in names[1:]] + [res["w_mod"][0]] + [o for nm, *_ in small for o in res[nm]]
    g_w_in, col0 = None, 0
    for i, (first, leaves, handle) in enumerate(s for s in sent if s[0].startswith("w_in")):
        half = W_IN_GRAD_CHUNKS[i][0]
        land, = _split_wait(handle, "grads_%s_wait" % first, done, srcs=w_in_grad[half])
        w_in_grad[half] = handle["srcs"]
        g_w_in = _sum_windows(land, "sum_windows%d" % i, col0, g_w_in)
        col0 += land.shape[1]
    major = lambda a: jnp.transpose(a, (2, 0, 1))
    outs = _adamw_rows3(g_w_in, major(w_in), major(m_w_in), major(v_w_in), "adamw_w_in")
    res["w_in"] = [jnp.transpose(o, (1, 2, 0)) for o in outs]

    order = ["c_ctx", "w_mod", "b_mod", "norm_pre1", "norm_post1", "norm_pre2", "norm_post2", "w_in", "hg_lb",
             "hg_onorm", "gla_w_gk", "gla_b_gk", "gla_onorm", "w_br_hg", "w_br_gla", "w_out", "w_ff_gate", "w_ff_up",
             "w_ff_down"]
    return (loss, grad_x, *[res[n][k] for k in range(4) for n in order])
```

```python
import functools

import jax
import jax.numpy as jnp
from jax import lax
from jax.experimental import pallas as pl
from jax.experimental.pallas import tpu as pltpu

F32 = jnp.float32
BF16 = jnp.bfloat16
HI = lax.Precision.HIGHEST

N_DEV = 8
D = 1024
CTX = 256
HW = 512
DH = 128
NH = 8
D_FF = 2816
EPS = 1e-6
GLA_NORM = 16.0
CHUNK = 64
TR = 256
NCT = CTX // TR
W_IN_COLS = 7168
MAIN0 = 0
LR0 = 4608
GW = 1152
GOFF = 32
GATE_HG0 = LR0
GATE_GLA0 = LR0 + D
LEVELS = (32, 16, 8)
EXP_CLAMP = 80.0
VMEM_LIMIT = 48 * 1024 * 1024

ADAM_LR, ADAM_B1, ADAM_B2, ADAM_EPS, ADAM_WD, ADAM_STEP = 0.001, 0.9, 0.999, 1e-08, 0.01, 10


def _cp(*sem):
    return pltpu.CompilerParams(dimension_semantics=sem, vmem_limit_bytes=VMEM_LIMIT)


def _sig(x):
    return jax.nn.sigmoid(x)


def _silu(x):
    return x * _sig(x)


def _dsilu(x):
    s = _sig(x)
    return s * (1.0 + x * (1.0 - s))


def _rstd(x):
    return lax.rsqrt(jnp.mean(x * x, axis=-1, keepdims=True) + EPS)


def _rms_bwd(a, y, r):
    return r * (a - y * (r * r) * jnp.mean(a * y, axis=-1, keepdims=True))


def _colsum(x):
    return jnp.sum(x, axis=0, keepdims=True)


def _dot(a, b, dims, precision=None):
    return lax.dot_general(a, b, (dims, ((), ())), preferred_element_type=F32, precision=precision)


NN = ((1,), (0,))
NT = ((1,), (1,))
TN = ((0,), (0,))

SCAN_HEADS_FWD = 4
SCAN_HEADS_BWD = 4


def _split_dot(m, x):
    mb = m.astype(BF16)
    x1 = x.astype(BF16)
    r1 = x - x1.astype(F32)
    x2 = r1.astype(BF16)
    x3 = (r1 - x2.astype(F32)).astype(BF16)
    return _dot(mb, x1, NN) + _dot(mb, x2, NN) + _dot(mb, x3, NN)


def _matmul(a, b, dims, out_dtype, name, tm, tn, tk, a_off=0, m_out=None):
    a_pair = isinstance(a, (tuple, list))
    as_ = list(a) if a_pair else [a]
    a = as_[0]
    pair = isinstance(b, (tuple, list))
    bs = list(b) if pair else [b]
    b1 = bs[0]
    rows = b1.shape[0] * len(bs)
    half = None
    if dims == NN:
        m, k, n = a.shape[0], rows, b1.shape[1]
        a_spec = pl.BlockSpec((tm, tk), lambda i, j, kk: (i, kk + a_off))
        half = b1.shape[0] // tk
        if a_pair:
            assert pair and a.shape[1] == b1.shape[0] and a_off == 0
            a_spec = [pl.BlockSpec((tm, tk), lambda i, j, kk: (i, jnp.minimum(kk, half - 1))),
                      pl.BlockSpec((tm, tk), lambda i, j, kk: (i, jnp.maximum(kk - half, 0)))]
        b_maps = [lambda i, j, kk: (kk, j)] if not pair else [
            lambda i, j, kk: (jnp.minimum(kk, half - 1), j), lambda i, j, kk: (jnp.maximum(kk - half, 0), j)]
        b_specs = [pl.BlockSpec((tk, tn), f) for f in b_maps]
        axis = 2
    elif dims == NT:
        m, k, n = a.shape[0], b1.shape[1], rows
        a_spec = pl.BlockSpec((tm, tk), lambda i, j, kk: (i, kk + a_off))
        half = b1.shape[0] // tn
        b_maps = [lambda i, j, kk: (j, kk)] if not pair else [
            lambda i, j, kk: (jnp.minimum(j, half - 1), kk), lambda i, j, kk: (jnp.maximum(j - half, 0), kk)]
        b_specs = [pl.BlockSpec((tn, tk), f) for f in b_maps]
        axis = 1
    else:
        assert not pair
        m, k = (a.shape[1] if m_out is None else m_out), a.shape[0]
        n = b1.shape[1]
        a_spec = pl.BlockSpec((tk, tm), lambda i, j, kk: (kk, i + a_off))
        b_specs = [pl.BlockSpec((tk, tn), lambda i, j, kk: (kk, j))]
    assert m % tm == 0 and n % tn == 0 and k % tk == 0, (name, m, n, k, tm, tn, tk)
    nk = k // tk
    nb = len(bs)
    na = len(as_)
    assert na == 1 or dims == NN

    def body(*refs):
        a_refs, refs = refs[:na], refs[na:]
        o_ref = refs[nb]
        if pair:
            bv = jnp.where(pl.program_id(axis) < half, refs[0][...], refs[1][...])
        else:
            bv = refs[0][...]
        av = a_refs[0][...] if na == 1 else jnp.where(pl.program_id(2) < half, a_refs[0][...], a_refs[1][...])
        part = _dot(av, bv, dims)
        if nk == 1:
            o_ref[...] = part.astype(o_ref.dtype)
            return
        acc_ref = refs[nb + 1]
        kk = pl.program_id(2)

        @pl.when(kk == 0)
        def _():
            acc_ref[...] = part

        @pl.when(kk > 0)
        def _():
            acc_ref[...] += part

        @pl.when(kk == nk - 1)
        def _():
            o_ref[...] = acc_ref[...].astype(o_ref.dtype)

    return pl.pallas_call(
        body,
        name=name,
        grid=(m // tm, n // tn, nk),
        in_specs=(a_spec if a_pair else [a_spec]) + b_specs,
        out_specs=pl.BlockSpec((tm, tn), lambda i, j, kk: (i, j)),
        out_shape=jax.ShapeDtypeStruct((m, n), out_dtype),
        scratch_shapes=[] if nk == 1 else [pltpu.VMEM((tm, tn), F32)],
        compiler_params=_cp("parallel", "parallel", "arbitrary"),
    )(*as_, *bs)


def _mm_gu_act(h, w_gate_t, w_up_t, name, tm):
    t = h.shape[0]
    tn = D_FF // 2

    def body(a_ref, bg_ref, bu_ref, u_ref, v_ref, act_ref):
        a = a_ref[...]
        u = _dot(a, bg_ref[...], NT)
        v = _dot(a, bu_ref[...], NT)
        u_ref[...] = u.astype(BF16)
        v_ref[...] = v.astype(BF16)
        act_ref[...] = (_silu(u) * v).astype(BF16)

    wspec = pl.BlockSpec((tn, D), lambda i, j: (j, 0))
    ospec = pl.BlockSpec((tm, tn), lambda i, j: (i, j))
    out = jax.ShapeDtypeStruct((t, D_FF), BF16)
    return pl.pallas_call(
        body, name=name, grid=(t // tm, D_FF // tn),
        in_specs=[pl.BlockSpec((tm, D), lambda i, j: (i, 0)), wspec, wspec],
        out_specs=[ospec] * 3, out_shape=[out] * 3,
        compiler_params=_cp("parallel", "parallel"),
    )(h, w_gate_t, w_up_t)


def _mm_down_dx_act(dy, w_down, u, v, name, tm):
    t = dy.shape[0]
    tn = D_FF // 2

    def body(a_ref, b_ref, u_ref, v_ref, du_ref, dv_ref):
        dact = _dot(a_ref[...], b_ref[...], NT)
        u = u_ref[...].astype(F32)
        du_ref[...] = (dact * v_ref[...].astype(F32) * _dsilu(u)).astype(BF16)
        dv_ref[...] = (dact * _silu(u)).astype(BF16)

    ospec = pl.BlockSpec((tm, tn), lambda i, j: (i, j))
    out = jax.ShapeDtypeStruct((t, D_FF), BF16)
    return pl.pallas_call(
        body, name=name, grid=(t // tm, D_FF // tn),
        in_specs=[pl.BlockSpec((tm, D), lambda i, j: (i, 0)), pl.BlockSpec((tn, D), lambda i, j: (j, 0)), ospec, ospec],
        out_specs=[ospec] * 2, out_shape=[out] * 2,
        compiler_params=_cp("parallel", "parallel"),
    )(dy, w_down, u, v)


def _row(c):
    return pl.BlockSpec((TR, c), lambda i: (i, 0))


def _rowcol(width, cb):
    return pl.BlockSpec((TR, width), lambda i: (i, cb))


def _full(shape):
    return pl.BlockSpec(shape, lambda i: (0,) * len(shape))


def _mod_row(mc_ref, mx_ref, k, is_ctx):
    return jnp.where(is_ctx, mc_ref[k:k + 1, :], mx_ref[k:k + 1, :])


def _z_specs():
    return [pl.BlockSpec((TR, D), lambda i: (jnp.minimum(i, NCT - 1), 0)),
            pl.BlockSpec((TR, D), lambda i: (jnp.maximum(i - NCT, 0), 0))]


def _z_tile(c_ref, x_ref, is_ctx):
    return jnp.where(is_ctx, c_ref[...], x_ref[...])


def _acc_row(ref, k, val):
    ref[k:k + 1, :] += val


def _acc_mod(ref, k, is_ctx, val):
    zero = jnp.zeros_like(val)
    ref[k:k + 1, :] += jnp.where(is_ctx, val, zero)
    ref[k + 1:k + 2, :] += jnp.where(is_ctx, zero, val)


def _prenorm(z, nw, modc, modx, i_shift, i_scale, name):
    t = z[0].shape[0] + z[1].shape[0]

    def body(zc_ref, zx_ref, nw_ref, mc_ref, mx_ref, h_ref):
        is_ctx = pl.program_id(0) < NCT
        x = _z_tile(zc_ref, zx_ref, is_ctx)
        n = x * _rstd(x) * nw_ref[...]
        h = n * (1.0 + _mod_row(mc_ref, mx_ref, i_scale, is_ctx)) + _mod_row(mc_ref, mx_ref, i_shift, is_ctx)
        h_ref[...] = h.astype(BF16)

    return pl.pallas_call(
        body, name=name, grid=(t // TR,),
        in_specs=_z_specs() + [_full((1, D)), _full((8, D)), _full((8, D))],
        out_specs=_row(D),
        out_shape=jax.ShapeDtypeStruct((t, D), BF16),
        compiler_params=_cp("parallel"),
    )(*z, nw, modc, modx)


def _hg_lb(lb_ref, d):
    a0 = lb_ref[0, d:d + 1, :]
    a1 = lb_ref[1, d:d + 1, :]
    mx = jnp.maximum(a0, a1)
    e0 = jnp.exp(a0 - mx)
    e1 = jnp.exp(a1 - mx)
    return e0 / (e0 + e1)


def _log_sigmoid(x):
    return jnp.minimum(x, 0.0) - jnp.log(1.0 + jnp.exp(-jnp.abs(x)))


def _gates_fwd(p, hg_lb, wgk, bgk):
    t = p.shape[0]
    seg = lambda j: _rowcol(HW, MAIN0 // HW + j)

    def body(hq_ref, hi_ref, hf_ref, hb_ref, gq_ref, gk_ref, gv_ref, lr_ref, lb_ref, wgk_ref, bgk_ref,
             q_ref, v_ref, kf_ref, kb_ref, gf_ref, gb_ref):
        q_ref[:, :HW] = _silu(hq_ref[...].astype(F32)).astype(BF16)
        q_ref[:, HW:] = (gq_ref[...].astype(F32) * (DH ** -0.5)).astype(BF16)
        v_ref[:, :HW] = hi_ref[...]
        v_ref[:, HW:] = gv_ref[...]
        xg = _dot(lr_ref[...].astype(BF16), wgk_ref[...], NN) + bgk_ref[...]
        for d, (raw_ref, k_ref, g_ref) in enumerate(((hf_ref, kf_ref, gf_ref), (hb_ref, kb_ref, gb_ref))):
            lbd = _hg_lb(lb_ref, d)
            f = lbd + (1.0 - lbd) * _sig(raw_ref[...].astype(F32))
            k_ref[:, :HW] = (1.0 - f).astype(BF16)
            k_ref[:, HW:] = gk_ref[...]
            g_ref[:, :HW] = jnp.log(f)
            g_ref[:, HW:] = _log_sigmoid(xg[:, d * HW:(d + 1) * HW]) * (1.0 / GLA_NORM)

    out = jax.ShapeDtypeStruct((t, D), F32)
    outb = jax.ShapeDtypeStruct((t, D), BF16)
    return pl.pallas_call(
        body, name="gates_fwd", grid=(t // TR,),
        in_specs=[seg(0), seg(1), seg(2), seg(3), seg(5), seg(6), seg(7), _rowcol(DH, LR0 // DH),
                  _full((2, 2, HW)), _full((DH, D)), _full((1, D))],
        out_specs=[_row(D)] * 6,
        out_shape=[outb] * 4 + [out] * 2,
        compiler_params=_cp("parallel"),
    )(p, p, p, p, p, p, p, p, hg_lb, wgk, bgk)


def _post_fwd(o_fw, o_bw, p, onw):
    t = o_fw.shape[0]

    def body(of_ref, ob_ref, g1_ref, g2_ref, w_ref, y_ref):
        for h in range(NH):
            sl = slice(h * DH, (h + 1) * DH)
            o = of_ref[:, sl] + ob_ref[:, sl]
            g_ref = g1_ref if h < NH // 2 else g2_ref
            gs = slice((h % (NH // 2)) * DH, (h % (NH // 2) + 1) * DH)
            n = o * _rstd(o) * w_ref[:, sl]
            y_ref[:, sl] = (n * _silu(g_ref[:, gs].astype(F32))).astype(BF16)

    return pl.pallas_call(
        body, name="post_fwd", grid=(t // TR,),
        in_specs=[_row(D), _row(D), _rowcol(HW, MAIN0 // HW + 4), _rowcol(HW, MAIN0 // HW + 8), _full((1, D))],
        out_specs=_row(D),
        out_shape=jax.ShapeDtypeStruct((t, D), BF16),
        compiler_params=_cp("parallel"),
    )(o_fw, o_bw, p, p, onw)


def _gate_window_specs(col0):
    return [_rowcol(HW, col0 // HW), _rowcol(HW, col0 // HW + 1), _rowcol(DH, (col0 + 2 * HW) // DH)]


def _gate_window(refs):
    return jnp.concatenate([r[...].astype(F32) for r in refs], axis=1)


def _branch_merge(y, w_hg, w_gla, p):
    t = y.shape[0]

    def body(y_ref, wh_ref, wg_ref, a0, a1, a2, b0, b1, b2, u1_ref, u2_ref, m_ref):
        u1 = _dot(y_ref[:, :HW], wh_ref[...], NN)
        u2 = _dot(y_ref[:, HW:], wg_ref[...], NN)
        u1_ref[...] = u1.astype(BF16)
        u2_ref[...] = u2.astype(BF16)
        m_ref[...] = (_sig(_gate_window((a0, a1, a2))) * u1 + _sig(_gate_window((b0, b1, b2))) * u2).astype(BF16)

    out = jax.ShapeDtypeStruct((t, GW), BF16)
    return pl.pallas_call(
        body, name="branch_merge", grid=(t // TR,),
        in_specs=[_row(D), _full((HW, GW)), _full((HW, GW))] + _gate_window_specs(GATE_HG0)
        + _gate_window_specs(GATE_GLA0),
        out_specs=[_row(GW)] * 3, out_shape=[out] * 3,
        compiler_params=_cp("parallel"),
    )(y, w_hg, w_gla, p, p, p, p, p, p)


def _mid_fwd(z, y1, nw_post, nw_pre, modc, modx):
    t = y1.shape[0]

    def body(zc_ref, zx_ref, y_ref, wpo_ref, wpr_ref, mc_ref, mx_ref, z1_ref, h_ref):
        is_ctx = pl.program_id(0) < NCT
        y = y_ref[...].astype(F32)
        z1 = _z_tile(zc_ref, zx_ref, is_ctx) + _mod_row(mc_ref, mx_ref, 2, is_ctx) * (y * _rstd(y) * wpo_ref[...])
        z1_ref[...] = z1
        n = z1 * _rstd(z1) * wpr_ref[...]
        h = n * (1.0 + _mod_row(mc_ref, mx_ref, 4, is_ctx)) + _mod_row(mc_ref, mx_ref, 3, is_ctx)
        h_ref[...] = h.astype(BF16)

    return pl.pallas_call(
        body, name="mid_fwd", grid=(t // TR,),
        in_specs=_z_specs() + [_row(D), _full((1, D)), _full((1, D)), _full((8, D)), _full((8, D))],
        out_specs=[_row(D), _row(D)],
        out_shape=[jax.ShapeDtypeStruct((t, D), F32), jax.ShapeDtypeStruct((t, D), BF16)],
        compiler_params=_cp("parallel"),
    )(*z, y1, nw_post, nw_pre, modc, modx)


def _final(z1, y2, target, nw, modc, modx):
    t = z1.shape[0]

    def body(z1_ref, y_ref, tg_ref, w_ref, mc_ref, mx_ref, dz_ref, dy_ref, loss_ref, sm_ref):
        i = pl.program_id(0)
        is_ctx = i < NCT

        @pl.when(i == 0)
        def _():
            loss_ref[...] = jnp.zeros_like(loss_ref)
            sm_ref[...] = jnp.zeros_like(sm_ref)

        g = _mod_row(mc_ref, mx_ref, 5, is_ctx)
        y = y_ref[...].astype(F32)
        r = _rstd(y)
        w = w_ref[...]
        yr = y * r
        n = yr * w
        e = z1_ref[...] + g * n - tg_ref[...]
        lat = jnp.where(is_ctx, 0.0, 1.0)
        loss_ref[...] += lat * _colsum(e * e)
        dz = e * (lat / D)
        dz_ref[...] = dz
        _acc_mod(sm_ref, 0, is_ctx, _colsum(dz * n))
        dn = dz * g
        _acc_row(sm_ref, 2, _colsum(dn * yr))
        dy_ref[...] = _rms_bwd(dn * w, y, r).astype(BF16)

    return pl.pallas_call(
        body, name="final", grid=(t // TR,),
        in_specs=[_row(D), _row(D), pl.BlockSpec((TR, D), lambda i: (jnp.maximum(i - NCT, 0), 0)),
                  _full((1, D)), _full((8, D)), _full((8, D))],
        out_specs=[_row(D), _row(D), _full((1, D)), _full((8, D))],
        out_shape=[jax.ShapeDtypeStruct((t, D), F32), jax.ShapeDtypeStruct((t, D), BF16),
                   jax.ShapeDtypeStruct((1, D), F32), jax.ShapeDtypeStruct((8, D), F32)],
        compiler_params=_cp("arbitrary"),
    )(z1, y2, target, nw, modc, modx)


def _mid_bwd(dh2, dz, z1, y1, nw_post, nw_pre, modc, modx):
    t = z1.shape[0]

    def body(dh_ref, dz_ref, z1_ref, y_ref, wpo_ref, wpr_ref, mc_ref, mx_ref, dzo_ref, dy_ref, sm_ref):
        i = pl.program_id(0)
        is_ctx = i < NCT

        @pl.when(i == 0)
        def _():
            sm_ref[...] = jnp.zeros_like(sm_ref)

        dh = dh_ref[...].astype(F32)
        z1 = z1_ref[...]
        r = _rstd(z1)
        zr = z1 * r
        wpr = wpr_ref[...]
        n = zr * wpr
        _acc_mod(sm_ref, 0, is_ctx, _colsum(dh))
        _acc_mod(sm_ref, 2, is_ctx, _colsum(dh * n))
        dn = dh * (1.0 + _mod_row(mc_ref, mx_ref, 4, is_ctx))
        _acc_row(sm_ref, 6, _colsum(dn * zr))
        dz1 = dz_ref[...] + _rms_bwd(dn * wpr, z1, r)
        dzo_ref[...] = dz1
        y = y_ref[...].astype(F32)
        r1 = _rstd(y)
        yr = y * r1
        wpo = wpo_ref[...]
        g = _mod_row(mc_ref, mx_ref, 2, is_ctx)
        _acc_mod(sm_ref, 4, is_ctx, _colsum(dz1 * (yr * wpo)))
        dn1 = dz1 * g
        _acc_row(sm_ref, 7, _colsum(dn1 * yr))
        dy_ref[...] = _rms_bwd(dn1 * wpo, y, r1).astype(BF16)

    return pl.pallas_call(
        body, name="mid_bwd", grid=(t // TR,),
        in_specs=[_row(D)] * 4 + [_full((1, D)), _full((1, D)), _full((8, D)), _full((8, D))],
        out_specs=[_row(D), _row(D), _full((8, D))],
        out_shape=[jax.ShapeDtypeStruct((t, D), F32), jax.ShapeDtypeStruct((t, D), BF16),
                   jax.ShapeDtypeStruct((8, D), F32)],
        compiler_params=_cp("arbitrary"),
    )(dh2, dz, z1, y1, nw_post, nw_pre, modc, modx)


def _pre_bwd(dh1, dz, z, nw, modc, modx):
    t = dh1.shape[0]

    def body(dh_ref, dz_ref, zc_ref, zx_ref, w_ref, mc_ref, mx_ref, dzo_ref, sm_ref):
        i = pl.program_id(0)
        is_ctx = i < NCT

        @pl.when(i == 0)
        def _():
            sm_ref[...] = jnp.zeros_like(sm_ref)

        dh = dh_ref[...].astype(F32)
        x = _z_tile(zc_ref, zx_ref, is_ctx)
        r = _rstd(x)
        xr = x * r
        w = w_ref[...]
        _acc_mod(sm_ref, 0, is_ctx, _colsum(dh))
        _acc_mod(sm_ref, 2, is_ctx, _colsum(dh * (xr * w)))
        dn = dh * (1.0 + _mod_row(mc_ref, mx_ref, 1, is_ctx))
        _acc_row(sm_ref, 4, _colsum(dn * xr))
        dzo_ref[...] = dz_ref[...] + _rms_bwd(dn * w, x, r)

    return pl.pallas_call(
        body, name="pre_bwd", grid=(t // TR,),
        in_specs=[_row(D)] * 2 + _z_specs() + [_full((1, D)), _full((8, D)), _full((8, D))],
        out_specs=[pl.BlockSpec((TR, D), lambda i: (jnp.maximum(i - NCT, 0), 0)), _full((8, D))],
        out_shape=[jax.ShapeDtypeStruct((t - CTX, D), F32), jax.ShapeDtypeStruct((8, D), F32)],
        compiler_params=_cp("arbitrary"),
    )(dh1, dz, *z, nw, modc, modx)


def _branch_merge_bwd(dm, p, u1, u2, w_hg, w_gla):
    t = dm.shape[0]

    def body(dm_ref, a0, a1, a2, b0, b1, b2, u1_ref, u2_ref, wh_ref, wg_ref, du1_ref, du2_ref, dg_ref, dyh_ref, dyg_ref):
        dm_ = dm_ref[...].astype(F32)
        s1 = _sig(_gate_window((a0, a1, a2)))
        s2 = _sig(_gate_window((b0, b1, b2)))
        du1 = (dm_ * s1).astype(BF16)
        du2 = (dm_ * s2).astype(BF16)
        du1_ref[...] = du1
        du2_ref[...] = du2
        dg_ref[:, :GW] = (dm_ * u1_ref[...].astype(F32) * s1 * (1.0 - s1)).astype(BF16)
        dg_ref[:, GW:] = (dm_ * u2_ref[...].astype(F32) * s2 * (1.0 - s2)).astype(BF16)
        dyh_ref[...] = _dot(du1, wh_ref[...], NT).astype(BF16)
        dyg_ref[...] = _dot(du2, wg_ref[...], NT).astype(BF16)

    return pl.pallas_call(
        body, name="branch_merge_bwd", grid=(t // TR,),
        in_specs=[_row(GW)] + _gate_window_specs(GATE_HG0) + _gate_window_specs(GATE_GLA0)
        + [_row(GW), _row(GW), _full((HW, GW)), _full((HW, GW))],
        out_specs=[_row(GW), _row(GW), _row(2 * GW), _row(HW), _row(HW)],
        out_shape=[jax.ShapeDtypeStruct((t, GW), BF16), jax.ShapeDtypeStruct((t, GW), BF16),
                   jax.ShapeDtypeStruct((t, 2 * GW), BF16), jax.ShapeDtypeStruct((t, HW), BF16),
                   jax.ShapeDtypeStruct((t, HW), BF16)],
        compiler_params=_cp("parallel"),
    )(dm, p, p, p, p, p, p, u1, u2, w_hg, w_gla)


def _post_bwd(dy_hg, dy_gla, o_fw, o_bw, p, onw):
    t = o_fw.shape[0]

    def body(d1_ref, d2_ref, of_ref, ob_ref, g1_ref, g2_ref, w_ref, do_ref, dg_ref, sm_ref):
        @pl.when(pl.program_id(0) == 0)
        def _():
            sm_ref[...] = jnp.zeros_like(sm_ref)

        for h in range(NH):
            sl = slice(h * DH, (h + 1) * DH)
            gs = slice((h % (NH // 2)) * DH, (h % (NH // 2) + 1) * DH)
            g_ref, d_ref = (g1_ref, d1_ref) if h < NH // 2 else (g2_ref, d2_ref)
            o = of_ref[:, sl] + ob_ref[:, sl]
            r = _rstd(o)
            orr = o * r
            w = w_ref[:, sl]
            gt = g_ref[:, gs].astype(F32)
            dy = d_ref[:, gs].astype(F32)
            dg_ref[:, sl] = (dy * (orr * w) * _dsilu(gt)).astype(BF16)
            dn = dy * _silu(gt)
            sm_ref[0:1, sl] += _colsum(dn * orr)
            do_ref[:, sl] = _rms_bwd(dn * w, o, r)

    return pl.pallas_call(
        body, name="post_bwd", grid=(t // TR,),
        in_specs=[_row(HW), _row(HW), _row(D), _row(D), _rowcol(HW, MAIN0 // HW + 4), _rowcol(HW, MAIN0 // HW + 8),
                  _full((1, D))],
        out_specs=[_row(D), _row(D), _full((8, D))],
        out_shape=[jax.ShapeDtypeStruct((t, D), F32), jax.ShapeDtypeStruct((t, D), BF16),
                   jax.ShapeDtypeStruct((8, D), F32)],
        compiler_params=_cp("arbitrary"),
    )(dy_hg, dy_gla, o_fw, o_bw, p, p, onw)


def _gates_bwd(p, hg_lb, wgk, bgk, dgm, dgo, dq_f, dq_b, dv_f, dv_b, dk_f, dk_b, dg_f, dg_b):
    t = p.shape[0]
    seg = lambda j: _rowcol(HW, MAIN0 // HW + j)

    def body(hq_ref, hf_ref, hb_ref, lr_ref, lb_ref, wgk_ref, bgk_ref, dgm_ref, dgo_ref,
             dqf_ref, dqb_ref, dvf_ref, dvb_ref, dkf_ref, dkb_ref, dgf_ref, dgb_ref,
             dp_ref, dlb_ref, dw_ref, db_ref):
        @pl.when(pl.program_id(0) == 0)
        def _():
            dlb_ref[...] = jnp.zeros_like(dlb_ref)
            dw_ref[...] = jnp.zeros_like(dw_ref)
            db_ref[...] = jnp.zeros_like(db_ref)

        c0 = MAIN0

        def put(j, val):
            dp_ref[:, c0 + j * HW:c0 + (j + 1) * HW] = val.astype(BF16)

        dq = dqf_ref[...].astype(F32) + dqb_ref[...].astype(F32)
        dv = dvf_ref[...].astype(F32) + dvb_ref[...].astype(F32)
        put(0, dq[:, :HW] * _dsilu(hq_ref[...].astype(F32)))
        put(1, dv[:, :HW])
        put(5, dq[:, HW:] * (DH ** -0.5))
        put(7, dv[:, HW:])
        put(6, dkf_ref[:, HW:].astype(F32) + dkb_ref[:, HW:].astype(F32))
        dp_ref[:, c0 + 4 * HW:c0 + 5 * HW] = dgo_ref[:, :HW]
        dp_ref[:, c0 + 8 * HW:c0 + 9 * HW] = dgo_ref[:, HW:]
        lr = lr_ref[...].astype(BF16)
        xg = _dot(lr, wgk_ref[...], NN) + bgk_ref[...]
        dxg = []
        for d, (raw_ref, dk_ref, dg_ref) in enumerate(((hf_ref, dkf_ref, dgf_ref), (hb_ref, dkb_ref, dgb_ref))):
            lbd = _hg_lb(lb_ref, d)
            s = _sig(raw_ref[...].astype(F32))
            f = lbd + (1.0 - lbd) * s
            df = dg_ref[:, :HW] / f - dk_ref[:, :HW].astype(F32)
            put(2 + d, df * (1.0 - lbd) * s * (1.0 - s))
            dlb_ref[d:d + 1, :] += _colsum(df * (1.0 - s)) * (lbd * (1.0 - lbd))
            dxg.append(dg_ref[:, HW:] * (1.0 / GLA_NORM) * _sig(-xg[:, d * HW:(d + 1) * HW]))
        dxg = jnp.concatenate(dxg, axis=1)
        db_ref[0:1, :] += _colsum(dxg)
        dxg_b = dxg.astype(BF16)
        dw_ref[...] += _dot(lr, dxg_b, TN)
        dlr = _dot(dxg_b, wgk_ref[...], NT)
        dp_ref[:, LR0:LR0 + DH] = (dlr + dgm_ref[:, :DH].astype(F32)).astype(BF16)
        dp_ref[:, LR0 + DH:GATE_GLA0] = dgm_ref[:, DH:D]
        dp_ref[:, GATE_GLA0:GATE_GLA0 + DH] = dgm_ref[:, D:GW] + dgm_ref[:, GW:GW + DH]
        dp_ref[:, GATE_GLA0 + DH:GATE_GLA0 + GW] = dgm_ref[:, GW + DH:]
        dp_ref[:, GATE_GLA0 + GW:] = jnp.zeros((TR, W_IN_COLS - GATE_GLA0 - GW), BF16)

    return pl.pallas_call(
        body, name="gates_bwd", grid=(t // TR,),
        in_specs=[seg(0), seg(2), seg(3), _rowcol(DH, LR0 // DH), _full((2, 2, HW)), _full((DH, D)), _full((1, D)),
                  _row(2 * GW), _row(D)] + [_row(D)] * 8,
        out_specs=[_row(W_IN_COLS), _full((8, HW)), _full((DH, D)), _full((8, D))],
        out_shape=[jax.ShapeDtypeStruct((t, W_IN_COLS), BF16), jax.ShapeDtypeStruct((8, HW), F32),
                   jax.ShapeDtypeStruct((DH, D), F32), jax.ShapeDtypeStruct((8, D), F32)],
        compiler_params=_cp("arbitrary"),
    )(p, p, p, p, hg_lb, wgk, bgk, dgm, dgo, dq_f, dq_b, dv_f, dv_b, dk_f, dk_b, dg_f, dg_b)


def _scan_consts(rev):
    r = lax.broadcasted_iota(jnp.int32, (CHUNK, CHUNK), 0)
    u = lax.broadcasted_iota(jnp.int32, (CHUNK, CHUNK), 1)
    rp = lax.broadcasted_iota(jnp.int32, (CHUNK, 1), 0)
    if rev:
        r, u, rp = CHUNK - 1 - r, CHUNK - 1 - u, CHUNK - 1 - rp
    tri = jnp.where(u <= r, 1.0, 0.0).astype(F32)
    tri_t = jnp.where(r <= u, 1.0, 0.0).astype(F32)
    lv = []
    for b in LEVELS:
        sh = b.bit_length() - 1
        pair = ((r >> sh) == (u >> sh) + 1) & (((u >> sh) & 1) == 0)
        pair_t = ((u >> sh) == (r >> sh) + 1) & (((r >> sh) & 1) == 0)
        tside = ((rp >> sh) & 1) == 1
        lv.append((pair, pair_t, tside, jnp.where(tside, 1.0, -1.0).astype(F32)))
    bd = LEVELS[-1].bit_length() - 1
    diag = ((r >> bd) == (u >> bd)) & (u <= r)
    diag_t = ((r >> bd) == (u >> bd)) & (r <= u)
    return tri, tri_t, lv, diag, diag_t


def _row_of(pos, rev):
    return CHUNK - 1 - pos if rev else pos


def _chunk_terms(cum, b_scr, consts, rev):
    _, _, lv, _, _ = consts
    terms = []
    for b, (_, _, _, sgn) in zip(LEVELS, lv):
        pieces = []
        for j in range(CHUNK // (2 * b)):
            row = _row_of(2 * b * j + b - 1, rev)
            pieces.append(jnp.broadcast_to(b_scr[row:row + 1, :], (2 * b, DH)))
        if rev:
            pieces = pieces[::-1]
        bnd = pieces[0] if len(pieces) == 1 else jnp.concatenate(pieces, axis=0)
        terms.append(jnp.exp((cum - bnd) * sgn))
    b = LEVELS[-1]
    pieces = []
    for j in range(CHUNK // b):
        if j == 0:
            pieces.append(jnp.zeros((b, DH), F32))
        else:
            row = _row_of(b * j - 1, rev)
            pieces.append(jnp.broadcast_to(b_scr[row:row + 1, :], (b, DH)))
    if rev:
        pieces = pieces[::-1]
    start = jnp.concatenate(pieces, axis=0)
    wq = jnp.exp(jnp.minimum(cum - start, 0.0))
    wk = jnp.exp(jnp.minimum(start - cum, EXP_CLAMP))
    terms.append((wq, wk))
    return terms


def _run_staged(units):
    live = list(units)
    while live:
        nxt = []
        for u in live:
            try:
                next(u)
                nxt.append(u)
            except StopIteration:
                pass
        live = nxt


SCAN_TB = 256
SCAN_CB = SCAN_TB // CHUNK


def _block_order(i, ntb, rev):
    nctx = CTX // SCAN_TB
    if not rev:
        return i
    return jnp.where(i < nctx, nctx - 1 - i, ntb - 1 - (i - nctx))


def _chunk_in_block(j, rev):
    return SCAN_CB - 1 - j if rev else j


def _scan_fwd(q, k, v, g, rev):
    t = q.shape[0]
    nc = t // CHUNK
    hpb = SCAN_HEADS_FWD

    def body(q_ref, k_ref, v_ref, g_ref, o_ref, st_ref, s_scr, b_scr):
        consts = _scan_consts(rev)
        _, _, lv, diag, _ = consts
        masks = [lvl[0] for lvl in lv] + [diag]

        @pl.when(pl.program_id(1) == 0)
        def _():
            s_scr[...] = jnp.zeros_like(s_scr)

        tri = consts[0]
        state = {hh: s_scr[hh] for hh in range(hpb)}

        def unit(hh, j):
            sl = slice(hh * DH, (hh + 1) * DH)
            c = _chunk_in_block(j, rev)
            rows = slice(c * CHUNK, (c + 1) * CHUNK)
            b_ref = b_scr.at[hh * SCAN_CB + j]
            qc, kc, vc, gc = q_ref[rows, sl], k_ref[rows, sl], v_ref[rows, sl], g_ref[rows, sl]
            cum = _split_dot(tri, gc)
            b_ref[...] = cum
            yield
            terms = _chunk_terms(cum, b_ref, consts, rev)
            qf, kf = qc.astype(F32), kc.astype(F32)
            xs = [(jnp.where(tside, qf, kf) * w).astype(BF16) for w, (_, _, tside, _) in zip(terms[:-1], lv)]
            qd, kd = (qf * terms[-1][0]).astype(BF16), (kf * terms[-1][1]).astype(BF16)
            tot = _colsum(gc)
            qe = (qf * jnp.exp(cum)).astype(BF16)
            ke = (kf * jnp.exp(tot - cum)).astype(BF16)
            vb = vc.astype(BF16)
            yield
            scs = [_dot(x, x, NT) for x in xs] + [_dot(qd, kd, NT)]
            kv = _dot(vb, ke, TN)
            yield
            a = jnp.zeros((CHUNK, CHUNK), F32)
            for sc, m in zip(scs, masks):
                a = a + jnp.where(m, sc, 0.0)
            o_intra = _dot(a.astype(BF16), vb, NN)
            yield
            st = state[hh]
            st_ref[hh, c] = st
            o_ref[rows, sl] = o_intra + _dot(qe, st.astype(BF16), NT)
            state[hh] = st * jnp.exp(tot) + kv
            yield

        _run_staged([unit(hh, j) for hh in range(hpb) for j in range(SCAN_CB)])
        for hh in range(hpb):
            s_scr[hh] = state[hh]

    ntb = t // SCAN_TB
    col = pl.BlockSpec((SCAN_TB, hpb * DH), lambda h, i: (_block_order(i, ntb, rev), h))
    return pl.pallas_call(
        body, name="scan_fwd_" + ("bw" if rev else "fw"), grid=(NH // hpb, ntb),
        in_specs=[col] * 4,
        out_specs=[col, pl.BlockSpec((hpb, SCAN_CB, DH, DH), lambda h, i: (h, _block_order(i, ntb, rev), 0, 0))],
        out_shape=[jax.ShapeDtypeStruct((t, D), F32), jax.ShapeDtypeStruct((NH, nc, DH, DH), F32)],
        scratch_shapes=[pltpu.VMEM((hpb, DH, DH), F32), pltpu.VMEM((hpb * SCAN_CB, CHUNK, DH), F32)],
        compiler_params=_cp("parallel", "arbitrary"),
    )(q, k, v, g)


def _scan_bwd(q, k, v, g, do, states, rev):
    t = q.shape[0]
    nc = t // CHUNK
    hpb = SCAN_HEADS_BWD

    def body(q_ref, k_ref, v_ref, g_ref, do_ref, st_ref, dq_ref, dk_ref, dv_ref, dg_ref, ds_scr, b_scr):
        consts = _scan_consts(rev)
        _, tri_t, lv, diag, diag_t = consts
        masks = [(lvl[0], lvl[1]) for lvl in lv] + [(diag, diag_t)]
        @pl.when(pl.program_id(1) == 0)
        def _():
            ds_scr[...] = jnp.zeros_like(ds_scr)

        tri = consts[0]
        dstate = {hh: ds_scr[hh] for hh in range(hpb)}

        def unit(hh, jj):
            sl = slice(hh * DH, (hh + 1) * DH)
            c = _chunk_in_block(SCAN_CB - 1 - jj, rev)
            rows = slice(c * CHUNK, (c + 1) * CHUNK)
            b_ref = b_scr.at[hh * SCAN_CB + jj]
            qc, kc, vc, gc = q_ref[rows, sl], k_ref[rows, sl], v_ref[rows, sl], g_ref[rows, sl]
            dob = do_ref[rows, sl].astype(BF16)
            vb = vc.astype(BF16)
            cum = _split_dot(tri, gc)
            b_ref[...] = cum
            da = _dot(dob, vb, NT)
            da_t = _dot(vb, dob, NT)
            yield
            terms = _chunk_terms(cum, b_ref, consts, rev)
            qf, kf = qc.astype(F32), kc.astype(F32)
            xs = [(jnp.where(tside, qf, kf) * w).astype(BF16) for w, (_, _, tside, _) in zip(terms[:-1], lv)]
            wqd, wkd = terms[-1]
            qdb, kdb = (qf * wqd).astype(BF16), (kf * wkd).astype(BF16)
            tot = _colsum(gc)
            e_tot = jnp.exp(tot)
            e_b = jnp.exp(cum)
            e_t = jnp.exp(tot - cum)
            qeb = (qf * e_b).astype(BF16)
            keb = (kf * e_t).astype(BF16)
            dsym = [(jnp.where(m, da, 0.0) + jnp.where(m_t, da_t, 0.0)).astype(BF16) for m, m_t in masks[:-1]]
            dad = (jnp.where(diag, da, 0.0).astype(BF16), jnp.where(diag_t, da_t, 0.0).astype(BF16))
            yield
            sym = [_dot(x, x, NT) for x in xs]
            dxs = [_dot(d, x, NN) for d, x in zip(dsym, xs)]
            at_d = _dot(kdb, qdb, NT)
            dqt_d = _dot(dad[0], kdb, NN)
            dkt_d = _dot(dad[1], qdb, NN)
            qd = _dot(dob, qeb, TN)
            yield
            a_t = jnp.where(diag_t, at_d, 0.0)
            dq = dqt_d * wqd
            dk = dkt_d * wkd
            db = dqt_d * qdb.astype(F32) - dkt_d * kdb.astype(F32)
            for s, dx, x, w, (_, m_t, tside, sgn) in zip(sym, dxs, xs, terms[:-1], lv):
                a_t = a_t + jnp.where(m_t, s, 0.0)
                dxw = dx * w
                dq = dq + jnp.where(tside, dxw, 0.0)
                dk = dk + jnp.where(tside, 0.0, dxw)
                db = db + (dx * x.astype(F32)) * sgn
            dv_intra = _dot(a_t.astype(BF16), dob, NN)
            st = st_ref[hh, c]
            stb = st.astype(BF16)
            dqe = _dot(dob, stb, NN)
            yield
            dst = dstate[hh]
            dstb = dst.astype(BF16)
            dstate[hh] = dst * e_tot + qd
            dv_ref[rows, sl] = (dv_intra + _dot(keb, dstb, NT)).astype(BF16)
            dke = _dot(vb, dstb, NN)
            yield
            qe = qeb.astype(F32)
            ke = keb.astype(F32)
            dq_ref[rows, sl] = (dq + dqe * e_b).astype(BF16)
            dk_ref[rows, sl] = (dk + dke * e_t).astype(BF16)
            db = db + dqe * qe - dke * ke
            dtot = _colsum(dstb.astype(F32) * stb.astype(F32)) * e_tot + _colsum(dke * ke)
            dg_ref[rows, sl] = _split_dot(tri_t, db) + dtot
            yield

        _run_staged([unit(hh, jj) for hh in range(hpb) for jj in range(SCAN_CB)])
        for hh in range(hpb):
            ds_scr[hh] = dstate[hh]

    ntb = t // SCAN_TB
    blk = lambda i: _block_order(ntb - 1 - i, ntb, rev)
    col = pl.BlockSpec((SCAN_TB, hpb * DH), lambda h, i: (blk(i), h))
    out = jax.ShapeDtypeStruct((t, D), F32)
    outb = jax.ShapeDtypeStruct((t, D), BF16)
    return pl.pallas_call(
        body, name="scan_bwd_" + ("bw" if rev else "fw"), grid=(NH // hpb, ntb),
        in_specs=[col] * 5 + [pl.BlockSpec((hpb, SCAN_CB, DH, DH), lambda h, i: (h, blk(i), 0, 0))],
        out_specs=[col] * 4,
        out_shape=[outb] * 3 + [out],
        scratch_shapes=[pltpu.VMEM((hpb, DH, DH), F32), pltpu.VMEM((hpb * SCAN_CB, CHUNK, DH), F32)],
        compiler_params=_cp("parallel", "arbitrary"),
    )(q, k, v, g, do, states)


W_IN_GRAD_CHUNKS = (("a", (0, 512)), ("b", (0, 256)), ("b", (256, 512)))
W_IN_REF = 6688
W_IN_PAD = 896
W_IN_PIECE = 256
W_IN_STAGES = (3, 4)


def _assemble_w_in(g, rows, prev, name):
    n, r, wp = g.shape
    tr = W_IN_PIECE
    tiles = wp // DH
    first = rows[0] // tr

    def body(g_ref, *refs):
        o_ref = refs[-1]
        lane = lax.broadcasted_iota(jnp.int32, (tr, DH), 1)
        for t in range(W_IN_COLS // DH):
            acc = None
            for j in range(n):
                c = DH * t - W_IN_SHARD * j
                if c <= -DH or c >= W_IN_SHARD:
                    continue
                k, s = divmod(c, DH)
                lo = g_ref[j, :, k * DH:(k + 1) * DH] if 0 <= k < tiles else None
                hi = g_ref[j, :, (k + 1) * DH:(k + 2) * DH] if s and 0 <= k + 1 < tiles else None
                if s:
                    zero = jnp.zeros((tr, DH), g.dtype)
                    lo = zero if lo is None else pltpu.roll(lo, DH - s, 1)
                    hi = zero if hi is None else pltpu.roll(hi, DH - s, 1)
                    part = jnp.where(lane < DH - s, lo, hi)
                else:
                    part = lo
                acc = part if acc is None else acc + part
            o_ref[:, t * DH:(t + 1) * DH] = jnp.zeros((tr, DH), g.dtype) if acc is None else acc

    held = [] if prev is None else [prev]
    return pl.pallas_call(
        body, name=name, grid=((rows[1] - rows[0]) // tr,),
        in_specs=[pl.BlockSpec((n, tr, wp), lambda i: (0, first + i, 0))] + [pl.BlockSpec(memory_space=pl.ANY)] * len(held),
        out_specs=pl.BlockSpec((tr, W_IN_COLS), lambda i: (first + i, 0)),
        out_shape=jax.ShapeDtypeStruct((r, W_IN_COLS), g.dtype),
        input_output_aliases={1: 0} if held else {},
        compiler_params=_cp("parallel"),
    )(g, *held)


def _gate_cols(w):
    return jnp.pad(w, ((0, 0), (GOFF, GW - GOFF - D)))


def _gate_rows(w):
    return jnp.pad(w, ((GOFF, GW - GOFF - D), (0, 0)))


def _layout_wgk(w):
    r = w.shape[1]
    top = jnp.concatenate([w[0], jnp.zeros_like(w[0])], axis=1)
    bot = jnp.concatenate([jnp.zeros_like(w[1]), w[1]], axis=1)
    return jnp.concatenate([top, bot, jnp.zeros((DH - 2 * r, D), w.dtype)], axis=0)


def _unlayout_wgk(d, r=16):
    return jnp.stack([d[:r, :HW], d[r:2 * r, HW:]])


def _local_step(z, target, modc, modx, norms, onw, hg_lb, wgk, bgk, get_w_in, get_mix, get_ffn, send):
    n_pre1, n_post1, n_pre2, n_post2 = norms
    t = z[0].shape[0] + z[1].shape[0]
    tm = 1152 if t % 1152 == 0 else 256
    h1 = _prenorm(z, n_pre1, modc, modx, 0, 1, "prenorm1")
    w_in = get_w_in(h1)
    p = _matmul(h1, w_in, NN, BF16, "mm_in", t, 1024, D)
    q, v, k_f, k_b, g_f, g_b = _gates_fwd(p, hg_lb, wgk, bgk)
    o_f, st_f = _scan_fwd(q, k_f, v, g_f, False)
    o_b, st_b = _scan_fwd(q, k_b, v, g_b, True)
    y = _post_fwd(o_f, o_b, p, onw)
    w_br_hg, w_br_gla, w_out = get_mix(y)
    u1, u2, merged = _branch_merge(y, w_br_hg, w_br_gla, p)
    y1 = _matmul(merged, w_out, NN, BF16, "mm_out", tm, 512, GW)
    z1, h2 = _mid_fwd(z, y1, n_post1, n_pre2, modc, modx)
    w_gu_t, w_down = get_ffn(h2)
    u, v_ff, act = _mm_gu_act(h2, w_gu_t[0], w_gu_t[1], "mm_gu", tm)
    y2 = _matmul(act, w_down, NN, BF16, "mm_down", t, 512, D_FF)
    dz, dy2, loss_vec, sm_final = _final(z1, y2, target, n_post2, modc, modx)
    du, dv_ff = _mm_down_dx_act(dy2, w_down, u, v_ff, "mm_down_dx", tm)
    d_w_down = _matmul(act, dy2, TN, BF16, "mm_down_dw", D_FF // 2, 1024, t)
    dh2 = _matmul((du, dv_ff), w_gu_t, NN, BF16, "mm_gu_dx", tm, 512, D_FF)
    d_w_gate_t = _matmul(du, h2, TN, BF16, "mm_gate_dw", D_FF // 2, 1024, t)
    d_w_up_t = _matmul(dv_ff, h2, TN, BF16, "mm_up_dw", D_FF // 2, 1024, t)
    dh2 = send(("w_down", "w_gate_t", "w_up_t"), (d_w_down, d_w_gate_t, d_w_up_t), dh2)
    dz, dy1, sm_mid = _mid_bwd(dh2, dz, z1, y1, n_post1, n_pre2, modc, modx)
    dmerged = _matmul(dy1, w_out, NT, BF16, "mm_out_dx", tm, GW, D)
    d_w_out = _matmul(merged, dy1, TN, BF16, "mm_out_dw", GW, 512, t)
    du1, du2, dgm, dy_hg, dy_gla = _branch_merge_bwd(dmerged, p, u1, u2, w_br_hg, w_br_gla)
    d_w_br_hg = _matmul(y, du1, TN, BF16, "mm_br_hg_dw", HW, GW, t, a_off=0, m_out=HW)
    d_w_br_gla = _matmul(y, du2, TN, BF16, "mm_br_gla_dw", HW, GW, t, a_off=1, m_out=HW)
    dy_hg = send(("w_out", "w_br_hg", "w_br_gla"), (d_w_out, d_w_br_hg, d_w_br_gla), dy_hg)
    do, dgo, sm_post = _post_bwd(dy_hg, dy_gla, o_f, o_b, p, onw)
    dq_f, dk_f, dv_f, dg_f = _scan_bwd(q, k_f, v, g_f, do, st_f, False)
    dq_b, dk_b, dv_b, dg_b = _scan_bwd(q, k_b, v, g_b, do, st_b, True)
    dp, d_lb, d_wgk, d_bgk = _gates_bwd(p, hg_lb, wgk, bgk, dgm, dgo, dq_f, dq_b, dv_f, dv_b, dk_f, dk_b, dg_f, dg_b)
    d_w_in_a = _matmul(h1, dp, TN, BF16, "mm_in_dw_a", 512, 1024, t, a_off=0, m_out=D // 2)
    dp = send(("w_in_a",), (d_w_in_a,), dp)
    d_w_in_b = _matmul(h1, dp, TN, BF16, "mm_in_dw_b", 512, 1024, t, a_off=1, m_out=D // 2)
    dp = send(("w_in_b",), (d_w_in_b,), dp)
    dh1 = _matmul(dp, w_in, NT, BF16, "mm_in_dx", tm, 512, W_IN_COLS // 2)
    grad_x, sm_pre = _pre_bwd(dh1, dz, z, n_pre1, modc, modx)
    return dict(loss_vec=loss_vec, grad_x=grad_x, sm_final=sm_final, sm_mid=sm_mid, sm_post=sm_post, sm_pre=sm_pre,
                d_lb=d_lb, d_wgk=d_wgk, d_bgk=d_bgk)


MESH = pl.DeviceIdType.MESH
ANY = pl.BlockSpec(memory_space=pl.ANY)
N_REL = N_DEV - 1


def _place():
    return lax.axis_index("x"), lax.axis_index("y"), lax.axis_index("c")


def _slot(p):
    return 4 * p[0] + 2 * p[1] + p[2]


HBM = pl.BlockSpec(memory_space=pltpu.HBM)
SEM = pl.BlockSpec(memory_space=pltpu.SEMAPHORE)
EFFECT = pltpu.SideEffectType.DATAFLOW_SIDE_EFFECTING


def _peer_of(x, y, c, k):
    flip = lambda v, bit: 1 - v if bit else v
    return flip(x, k & 4), flip(y, k & 2), flip(c, k & 1)


def _view_whole(src, slot):
    return src


def _view_near(src, slot):
    return src


_view_near.peers = (1, 2, 4, 6)


def _view_near_rows(rows):
    def view(src, slot):
        return src.at[pl.ds(rows[0], rows[1] - rows[0])]
    view.peers = _view_near.peers
    view.land = lambda land, slot: land.at[slot, pl.ds(rows[0], rows[1] - rows[0])]
    return view


def _view_block(src, slot):
    return src.at[slot]


W_IN_SHARD = W_IN_REF // N_DEV


def _view_window(rows):
    def view(src, slot):
        col0 = pl.multiple_of((W_IN_SHARD * slot // DH) * DH, DH)
        return src.at[pl.ds(rows[0], rows[1] - rows[0]), pl.ds(col0, D)]
    return view


def _split_copies(view, srcs, lands, send_sems, recv_sems, local_sems):
    x, y, c = _place()
    me = _slot((x, y, c))
    into = getattr(view, "land", lambda land, slot: land.at[slot])
    local, sends, waits = [], [], []
    for a, (src, land) in enumerate(zip(srcs, lands)):
        local.append(pltpu.make_async_copy(view(src, me), into(land, me), local_sems.at[a]))
        for k in getattr(view, "peers", range(1, N_DEV)):
            peer = _peer_of(x, y, c, k)
            mine = view(src, _slot(peer))
            sems = dict(send_sem=send_sems.at[N_REL * a + k - 1], recv_sem=recv_sems.at[N_REL * a + k - 1],
                        device_id=peer, device_id_type=MESH)
            sends.append(pltpu.make_async_remote_copy(src_ref=mine, dst_ref=into(land, me), **sems))
            waits.append(pltpu.make_async_remote_copy(src_ref=mine, dst_ref=into(land, _slot(peer)), **sems))
    return local, sends, waits


def _split_start(groups, name, after):
    built = []
    for view, srcs, lands in groups:
        lands = [lax.empty(l, s.dtype) if isinstance(l, tuple) else l for l, s in zip(lands, srcs)]
        built.append((view, list(srcs), lands))
    bufs = [b for _, srcs, lands in built for b in srcs + lands]
    nb, ng = len(bufs), len(built)

    def body(*refs):
        buf_refs, sem_refs, token = refs[:nb], refs[nb + 1:nb + 1 + 3 * ng], refs[-1]
        pos = 0
        for i, (view, srcs, _) in enumerate(built):
            n = len(srcs)
            local, sends, _ = _split_copies(view, buf_refs[pos:pos + n], buf_refs[pos + n:pos + 2 * n],
                                            *sem_refs[3 * i:3 * i + 3])
            pos += 2 * n
            for cp in local + sends:
                cp.start()
        token[...] = jnp.zeros_like(token)

    sems = []
    for _, srcs, _ in built:
        n = len(srcs)
        sems += [pltpu.SemaphoreType.DMA((N_REL * n,)), pltpu.SemaphoreType.DMA((N_REL * n,)),
                 pltpu.SemaphoreType.DMA((n,))]
    hbm = lambda a: pltpu.with_memory_space_constraint(a, pltpu.HBM)
    out = pl.pallas_call(
        body, name=name,
        out_shape=(*sems, *[pltpu.HBM(b.shape, b.dtype) for b in bufs], jax.ShapeDtypeStruct((8, DH), F32)),
        in_specs=[HBM] * nb + [ANY],
        out_specs=(*([SEM] * (3 * ng)), *([HBM] * nb), pl.BlockSpec(memory_space=pltpu.VMEM)),
        input_output_aliases={i: 3 * ng + i for i in range(nb)},
        compiler_params=pltpu.CompilerParams(has_side_effects=EFFECT),
    )(*[hbm(b) for b in bufs], after)
    handles, pos = [], 3 * ng
    for i, (view, srcs, _) in enumerate(built):
        n = len(srcs)
        handles.append(dict(view=view, n=n, sems=out[3 * i:3 * i + 3], srcs=list(out[pos:pos + n]),
                            lands=list(out[pos + n:pos + 2 * n])))
        pos += 2 * n
    return handles, out[-1]


def _split_wait(handle, name, after, srcs=None, lands=None):
    view, n, sems = handle["view"], handle["n"], handle["sems"]
    srcs = handle["srcs"] if srcs is None else srcs
    lands = handle["lands"] if lands is None else lands
    afters = list(after) if isinstance(after, (list, tuple)) else [after]

    def body(*refs):
        src_refs, land_refs = refs[:n], refs[n:2 * n]
        send_sems, recv_sems, local_sems = refs[2 * n:2 * n + 3]
        local, _, waits = _split_copies(view, src_refs, land_refs, send_sems, recv_sems, local_sems)
        for cp in waits:
            cp.wait_send()
            cp.wait_recv()
        for cp in local:
            cp.wait()

    out = pl.pallas_call(
        body, name=name,
        out_shape=(*[pltpu.HBM(s.shape, s.dtype) for s in srcs], *[pltpu.HBM(l.shape, l.dtype) for l in lands]),
        in_specs=[HBM] * (2 * n) + [SEM, SEM, SEM] + [ANY] * len(afters),
        out_specs=tuple([HBM] * (2 * n)),
        input_output_aliases={i: i for i in range(2 * n)},
        compiler_params=pltpu.CompilerParams(has_side_effects=EFFECT),
    )(*srcs, *lands, *sems, *afters)
    handle["srcs"] = list(out[:n])
    return list(out[n:])


def _tie(x, token, name):
    def body(x_ref, t_ref, o_ref):
        pass

    return pl.pallas_call(
        body, name=name, out_shape=jax.ShapeDtypeStruct(x.shape, x.dtype),
        in_specs=[ANY, ANY], out_specs=ANY, input_output_aliases={0: 0},
    )(x, token)


def _forward_to_sibling(land, name, rows):
    def body(land_ref, out_ref, send_sems, recv_sems):
        x, y, c = _place()
        sibling = (x, y, 1 - c)
        chips = [(1 - x, y), (x, 1 - y), (1 - x, 1 - y)]
        piece = pl.ds(rows[0], rows[1] - rows[0])

        def copy(j, core):
            blk = _slot((*chips[j], core))
            return pltpu.make_async_remote_copy(src_ref=land_ref.at[blk, piece], dst_ref=out_ref.at[blk, piece],
                                                send_sem=send_sems.at[j], recv_sem=recv_sems.at[j],
                                                device_id=sibling, device_id_type=MESH)

        sends = [copy(j, c) for j in range(3)]
        for cp in sends:
            cp.start()
        for j in range(3):
            copy(j, 1 - c).wait_recv()
        for cp in sends:
            cp.wait_send()

    return pl.pallas_call(
        body, name=name, in_specs=[ANY], out_specs=ANY, input_output_aliases={0: 0},
        out_shape=jax.ShapeDtypeStruct(land.shape, land.dtype),
        scratch_shapes=[pltpu.SemaphoreType.DMA((3,)), pltpu.SemaphoreType.DMA((3,))],
    )(land)


def _mod_fwd(a, w, b):
    def body(a_ref, w_ref, b_ref, o_ref):
        o_ref[...] = _dot(_silu(a_ref[...]), w_ref[...], NN, precision=HI) + b_ref[...]

    return pl.pallas_call(
        body, name="mod_fwd", out_shape=jax.ShapeDtypeStruct((a.shape[0], w.shape[1]), F32),
        compiler_params=pltpu.CompilerParams(vmem_limit_bytes=VMEM_LIMIT),
    )(a, w, b)


def _mod_bwd(a, d, w):
    def body(a_ref, d_ref, w_ref, dw_ref, dc_ref):
        av = a_ref[...]
        dv = d_ref[...]
        dw_ref[...] = _dot(_silu(av), dv, TN, precision=HI)
        da = _dot(dv[0:8, :], w_ref[...], NT, precision=HI) * _dsilu(av[0:8, :])
        row = lax.broadcasted_iota(jnp.int32, da.shape, 0)
        dc_ref[...] = jnp.where(row == 0, da, 0.0)

    return pl.pallas_call(
        body, name="mod_bwd",
        out_shape=[jax.ShapeDtypeStruct(w.shape, F32), jax.ShapeDtypeStruct((8, w.shape[0]), F32)],
        compiler_params=pltpu.CompilerParams(vmem_limit_bytes=VMEM_LIMIT),
    )(a, d, w)


def _sum_devices(g):
    def body(g_ref, o_ref):
        acc = g_ref[0]
        for i in range(1, g.shape[0]):
            acc = acc + g_ref[i]
        o_ref[...] = acc

    return pl.pallas_call(body, name="sum_devices_%d" % g.shape[1],
                          out_shape=jax.ShapeDtypeStruct(g.shape[1:], F32))(g)


def _sum_windows(g, name):
    n, r, c = g.shape
    tr = 128

    def body(g_ref, o_ref):
        x, y, cc = _place()
        lane0 = (W_IN_SHARD * _slot((x, y, cc))) % DH
        acc = g_ref[0].astype(F32)
        for i in range(1, n):
            acc = acc + g_ref[i].astype(F32)
        o_ref[...] = pltpu.roll(acc, (c - lane0) % c, 1).T

    return pl.pallas_call(
        body, name=name, grid=(r // tr,),
        in_specs=[pl.BlockSpec((n, tr, c), lambda i: (0, i, 0))],
        out_specs=pl.BlockSpec((c, tr), lambda i: (0, i)),
        out_shape=jax.ShapeDtypeStruct((c, r), F32),
        compiler_params=_cp("parallel"),
    )(g)


def _adam_rows(r, c, n):
    budget = 10 * 1024 * 1024
    best = None
    for tr in range(16, r + 1, 16):
        if r % tr == 0 and tr * c * (2 * n + 28) <= budget:
            best = tr
    return best if best is not None else r


def _adamw(g, w, m, v, name):
    n, r, c = g.shape
    tr = _adam_rows(r, c, n)
    bc1 = 1.0 - ADAM_B1 ** ADAM_STEP
    bc2 = 1.0 - ADAM_B2 ** ADAM_STEP

    def body(g_ref, w_ref, m_ref, v_ref, go_ref, d_ref, mo_ref, vo_ref):
        grad = g_ref[0].astype(F32)
        for i in range(1, n):
            grad = grad + g_ref[i].astype(F32)
        go_ref[...] = grad
        m_new = ADAM_B1 * m_ref[...] + (1.0 - ADAM_B1) * grad
        v_new = ADAM_B2 * v_ref[...] + (1.0 - ADAM_B2) * (grad * grad)
        mo_ref[...] = m_new
        vo_ref[...] = v_new
        d_ref[...] = -ADAM_LR * ((m_new / bc1) / (jnp.sqrt(v_new / bc2) + ADAM_EPS) + ADAM_WD * w_ref[...])

    blk = pl.BlockSpec((tr, c), lambda i: (i, 0))
    out = jax.ShapeDtypeStruct((r, c), F32)
    return pl.pallas_call(
        body, name=name, grid=(r // tr,),
        in_specs=[pl.BlockSpec((n, tr, c), lambda i: (0, i, 0)), blk, blk, blk],
        out_specs=[blk] * 4, out_shape=[out] * 4,
        compiler_params=_cp("parallel"),
    )(g, w, m, v)


ADAM_ROWS3 = 168


def _adam_math(grad, w, m, v):
    bc1 = 1.0 - ADAM_B1 ** ADAM_STEP
    bc2 = 1.0 - ADAM_B2 ** ADAM_STEP
    m_new = ADAM_B1 * m + (1.0 - ADAM_B1) * grad
    v_new = ADAM_B2 * v + (1.0 - ADAM_B2) * (grad * grad)
    delta = -ADAM_LR * ((m_new / bc1) / (jnp.sqrt(v_new / bc2) + ADAM_EPS) + ADAM_WD * w)
    return delta, m_new, v_new


def _adamw_rows3(g, w3, m3, v3, name, cols, prev):
    r, _, _ = w3.shape
    c = cols[1] - cols[0]
    n = ADAM_ROWS3
    starts = list(range(0, r - n, n)) + [r - n]
    held = [] if prev is None else list(prev)

    def body(g_hbm, w_hbm, m_hbm, v_hbm, *refs):
        go_hbm, d_hbm, mo_hbm, vo_hbm, gbuf, ibuf, obuf, in_sems, out_sems = refs[len(held):]
        part = lambda h, r0: h.at[pl.ds(r0, n), 0, pl.ds(cols[0], c)]

        def fetch(p):
            r0, slot = starts[p], p % 2
            g0 = (r0 // 8) * 8
            cps = [pltpu.make_async_copy(g_hbm.at[pl.ds(g0, n + 8)], gbuf.at[slot], in_sems.at[slot, 0])]
            cps += [pltpu.make_async_copy(part(h, r0), ibuf.at[slot, k], in_sems.at[slot, 1 + k])
                    for k, h in enumerate((w_hbm, m_hbm, v_hbm))]
            for cp in cps:
                cp.start()
            return cps

        pending, outs = fetch(0), []
        for p, r0 in enumerate(starts):
            slot = p % 2
            nxt = fetch(p + 1) if p + 1 < len(starts) else []
            for cp in pending:
                cp.wait()
            grad = gbuf[slot, pl.ds(r0 - (r0 // 8) * 8, n), :]
            delta, m_new, v_new = _adam_math(grad, ibuf[slot, 0], ibuf[slot, 1], ibuf[slot, 2])
            for cp in outs:
                cp.wait()
            for k, val in enumerate((grad, delta, m_new, v_new)):
                obuf[slot, k] = val
            outs = [pltpu.make_async_copy(obuf.at[slot, k], part(h, r0), out_sems.at[slot, k])
                    for k, h in enumerate((go_hbm, d_hbm, mo_hbm, vo_hbm))]
            for cp in outs:
                cp.start()
            pending = nxt
        for cp in outs:
            cp.wait()

    out = jax.ShapeDtypeStruct(w3.shape, F32)
    return pl.pallas_call(
        body, name=name, in_specs=[ANY] * (4 + len(held)), out_specs=[ANY] * 4, out_shape=[out] * 4,
        input_output_aliases={4 + k: k for k in range(len(held))},
        scratch_shapes=[pltpu.VMEM((2, n + 8, c), F32), pltpu.VMEM((2, 3, n, c), F32), pltpu.VMEM((2, 4, n, c), F32),
                        pltpu.SemaphoreType.DMA((2, 4)), pltpu.SemaphoreType.DMA((2, 4))],
        compiler_params=pltpu.CompilerParams(vmem_limit_bytes=VMEM_LIMIT),
    )(g, w3, m3, v3, *held)


def kernel(x, c, ctx, c_ctx, w_mod, b_mod, norm_pre1, norm_post1, norm_pre2, norm_post2, w_in, hg_lb, hg_onorm, gla_w_gk, gla_b_gk, gla_onorm, w_br_hg, w_br_gla, w_out, w_ff_gate, w_ff_up, w_ff_down, loss_target, m_c_ctx, m_w_mod, m_b_mod, m_norm_pre1, m_norm_post1, m_norm_pre2, m_norm_post2, m_w_in, m_hg_lb, m_hg_onorm, m_gla_w_gk, m_gla_b_gk, m_gla_onorm, m_w_br_hg, m_w_br_gla, m_w_out, m_w_ff_gate, m_w_ff_up, m_w_ff_down, v_c_ctx, v_w_mod, v_b_mod, v_norm_pre1, v_norm_post1, v_norm_pre2, v_norm_post2, v_w_in, v_hg_lb, v_hg_onorm, v_gla_w_gk, v_gla_b_gk, v_gla_onorm, v_w_br_hg, v_w_br_gla, v_w_out, v_w_ff_gate, v_w_ff_up, v_w_ff_down):
    xi, yi, ci = lax.axis_index("x"), lax.axis_index("y"), lax.axis_index("c")
    me = 4 * xi + 2 * yi + ci
    t = CTX + x.shape[1]

    w_in_pieces, w_in_state = [], {}

    def w_in_piece(i):
        return (_view_near_rows((i * W_IN_PIECE, (i + 1) * W_IN_PIECE)), w_in_state["src"], w_in_state["land"])

    def started_w_in(handle):
        w_in_state.update(src=handle["srcs"], land=handle["lands"])
        w_in_pieces.append(handle)

    tr_ = lambda a: jnp.swapaxes(a[0], 0, 1)
    w_in_bf = jnp.pad(w_in[0].astype(BF16), ((0, 0), (0, W_IN_PAD - W_IN_SHARD)))
    w_in_state.update(src=[w_in_bf], land=[lax.empty((N_DEV,) + w_in_bf.shape, BF16)])
    gathered = lambda arrs: [(N_DEV,) + a.shape for a in arrs]
    whole = lambda arrs: (_view_whole, arrs, gathered(arrs))
    small_in = [c, hg_lb, gla_w_gk[0], gla_b_gk[0]]
    (small_handle, piece), tok = _split_start([whole(small_in), w_in_piece(0)], "ag_small_start", c)
    started_w_in(piece)
    c_all, lb_g, wgk_g, bgk_g = _split_wait(small_handle, "ag_small_wait", tok)
    big = [w_in[0], w_br_hg[0], w_br_gla[0], w_out[0], tr_(w_ff_gate), tr_(w_ff_up), w_ff_down[0]]
    big_bf = [None] + [w.astype(BF16) for w in big[1:]]
    cols = lambda g: jnp.transpose(g, (1, 0, 2)).reshape(g.shape[1], N_DEV * g.shape[2])

    def get_w_in(after):
        w_full, first = None, 0
        for s, last in enumerate(W_IN_STAGES):
            for i in range(first, last):
                land = _split_wait(w_in_pieces[i], "ag_w_in_wait%d" % i, after if w_full is None else [after, w_full],
                                   srcs=w_in_state["src"], lands=w_in_state["land"])
                w_in_state.update(src=w_in_pieces[i]["srcs"], land=land)
            rows = (first * W_IN_PIECE, last * W_IN_PIECE)
            w_in_state["land"] = [_forward_to_sibling(w_in_state["land"][0], "ag_w_in_forward%d" % s, rows)]
            w_full = _assemble_w_in(w_in_state["land"][0], rows, w_full, "assemble_w_in%d" % s)
            first = last
        return w_full

    def get_mix(after):
        g_brh, g_brg, g_out = _split_wait(mix_handle, "ag_mix_wait", after)
        return _gate_cols(cols(g_brh)), _gate_cols(cols(g_brg)), _gate_rows(g_out.reshape(D, D))

    def get_ffn(after):
        g_gate, g_up, g_down = _split_wait(ffn_handle, "ag_ffn_wait", after)
        return (g_gate.reshape(D_FF, D), g_up.reshape(D_FF, D)), g_down.reshape(D_FF, D)

    hg_lb_full = jnp.transpose(lb_g, (1, 2, 0, 3)).reshape(2, 2, HW)
    wgk_k = _layout_wgk(jnp.transpose(wgk_g, (1, 2, 0, 3)).reshape(2, 16, HW)).astype(BF16)
    bgk_k = jnp.transpose(bgk_g, (1, 0, 2)).reshape(1, D)
    onw = jnp.concatenate([jnp.tile(hg_onorm, (1, NH // 2)), jnp.tile(gla_onorm, (1, NH // 2))], axis=1)

    n_mod = w_mod.shape[2]
    a9 = jnp.concatenate([c_ctx[None], c_all[:, 0], jnp.zeros((16 - 1 - N_DEV, D), F32)], axis=0)
    b_loc = lax.dynamic_slice(b_mod, (0, me * n_mod), (1, n_mod))
    s_loc = _mod_fwd(a9, w_mod[0], b_loc)
    (mod_handle, piece), tok = _split_start([whole([s_loc]), w_in_piece(1)], "ag_mod_start", s_loc)
    started_w_in(piece)
    for i in range(2, D // W_IN_PIECE):
        (piece,), tok = _split_start([w_in_piece(i)], "ag_w_in_start%d" % i, tok)
        started_w_in(piece)
    s_all, = _split_wait(mod_handle, "ag_mod_wait", tok)
    mod_all = jnp.transpose(s_all, (1, 0, 2)).reshape(16, N_DEV * n_mod)
    pad8 = lambda m: jnp.concatenate([m.reshape(6, D), jnp.zeros((2, D), F32)], axis=0)
    modc = pad8(mod_all[0])
    modx = pad8(lax.dynamic_slice(mod_all, (1 + me, 0), (1, N_DEV * n_mod))[0])

    (mix_handle, ffn_handle), tok = _split_start([whole(big_bf[1:4]), whole(big_bf[4:])], "ag_big_start", s_all)

    z = (ctx[0], x[0])
    modx = _tie(modx, tok, "tie_mod")
    norms = (norm_pre1, norm_post1, norm_pre2, norm_post2)
    shard = lambda d: jnp.transpose(d.reshape(d.shape[0], N_DEV, -1), (1, 0, 2)).astype(BF16)
    rowshard = lambda d: d.reshape(N_DEV, d.shape[0] // N_DEV, d.shape[1]).astype(BF16)
    sent, w_in_grad = [], {}

    def w_in_chunk(i):
        half, rows = W_IN_GRAD_CHUNKS[i]
        return (_view_window(rows), w_in_grad[half], [(N_DEV, rows[1] - rows[0], D)])

    def sent_w_in(i, handle):
        w_in_grad[W_IN_GRAD_CHUNKS[i][0]] = handle["srcs"]
        sent.append(("w_in%d" % i, ["w_in#%d" % i], handle))

    def send(names, grads, x_after):
        if names == ("w_in_a",):
            w_in_grad["a"] = list(grads)
            (handle,), tok = _split_start([w_in_chunk(0)], "grads_w_in0_start", x_after)
            sent_w_in(0, handle)
            return _tie(x_after, tok, "tie_w_in0")
        if names == ("w_in_b",):
            w_in_grad["b"] = list(grads)
            return x_after
        arrs, leaves = [], []
        for nm, g in zip(names, grads):
            if nm in ("w_gate_t", "w_up_t"):
                arrs.append(rowshard(g))
                leaves.append({"w_gate_t": "w_ff_gate", "w_up_t": "w_ff_up"}[nm])
            elif nm == "w_down":
                arrs.append(rowshard(g))
                leaves.append("w_ff_down")
            elif nm == "w_out":
                arrs.append(rowshard(g[GOFF:GOFF + D]))
                leaves.append(nm)
            else:
                arrs.append(shard(g[:, GOFF:GOFF + D]))
                leaves.append(nm)
        (handle,), tok = _split_start([(_view_block, arrs, [a.shape for a in arrs])], "grads_%s_start" % names[0],
                                      x_after)
        sent.append((names[0], leaves, handle))
        return _tie(x_after, tok, "tie_" + names[0])

    r = _local_step(z, loss_target[0], modc, modx, norms, onw, hg_lb_full, wgk_k, bgk_k,
                    get_w_in, get_mix, get_ffn, send)
    grad_x = r["grad_x"][None]

    sm_pre, sm_mid, sm_fin = r["sm_pre"], r["sm_mid"], r["sm_final"]
    dmodc = jnp.stack([sm_pre[0], sm_pre[2], sm_mid[4], sm_mid[0], sm_mid[2], sm_fin[0]]).reshape(-1)
    dmodx = jnp.stack([sm_pre[1], sm_pre[3], sm_mid[5], sm_mid[1], sm_mid[3], sm_fin[1]]).reshape(-1)
    on = r["sm_post"][0].reshape(NH, DH)
    pieces = [dmodc, dmodx, sm_pre[4], sm_mid[7], sm_mid[6], sm_fin[2], on[:NH // 2].sum(0), on[NH // 2:].sum(0),
              r["d_lb"][:2].reshape(-1), _unlayout_wgk(r["d_wgk"]).reshape(-1), r["d_bgk"][0]]
    loss_local = (0.5 / D) * jnp.sum(r["loss_vec"])
    pieces.append(jnp.concatenate([loss_local.reshape(1), jnp.zeros((DH - 1,), F32)]))
    sizes = [p.shape[0] for p in pieces]
    pack = jnp.concatenate(pieces).reshape(-1, DH)
    moms = [(m_w_in, v_w_in), (m_w_br_hg, v_w_br_hg), (m_w_br_gla, v_w_br_gla), (m_w_out, v_w_out),
            (m_w_ff_gate, v_w_ff_gate), (m_w_ff_up, v_w_ff_up), (m_w_ff_down, v_w_ff_down)]
    names = ["w_in", "w_br_hg", "w_br_gla", "w_out", "w_ff_gate", "w_ff_up", "w_ff_down"]
    wmv = {nm: (w, m, v) for nm, w, (m, v) in zip(names, big, moms)}
    res = {}

    def update(nm):
        w, m, v = wmv[nm]
        if nm in ("w_ff_gate", "w_ff_up"):
            outs = _adamw(recv[nm], w, tr_(m), tr_(v), "adamw_" + nm)
            res[nm] = [jnp.swapaxes(o, 0, 1)[None] for o in outs]
        else:
            res[nm] = [o[None] for o in _adamw(recv[nm], w, m[0], v[0], "adamw_" + nm)]

    (small_handle, handle), tok = _split_start([whole([pack]), w_in_chunk(1)], "small_grads_start", pack)
    sent_w_in(1, handle)
    recv = {}
    for first, leaves, handle in sent:
        if not first.startswith("w_in"):
            recv.update(zip(leaves, _split_wait(handle, "grads_%s_wait" % first, tok)))
    update("w_ff_gate")
    update("w_ff_up")
    pack_all, = _split_wait(small_handle, "small_grads_wait", [res["w_ff_gate"][0], res["w_ff_up"][0]])
    tot = _sum_devices(pack_all).reshape(-1)
    offs = [sum(sizes[:i]) for i in range(len(sizes))]
    part = lambda i: tot[offs[i]:offs[i] + sizes[i]]
    dmodc_t, dmodx_t = part(0), part(1)
    g_b_mod = (dmodc_t + dmodx_t)[None]
    g_norms = [part(i)[None] for i in (2, 3, 4, 5)]
    g_hg_on, g_gla_on = part(6)[None], part(7)[None]
    lb0 = lax.dynamic_slice(part(8).reshape(2, HW), (0, me * (HW // N_DEV)), (2, HW // N_DEV))
    g_hg_lb = jnp.stack([lb0, -lb0])
    g_wgk = lax.dynamic_slice(part(9).reshape(2, 16, HW), (0, 0, me * (HW // N_DEV)), (2, 16, HW // N_DEV))[None]
    g_bgk = lax.dynamic_slice(part(10).reshape(2, HW), (0, me * (HW // N_DEV)), (2, HW // N_DEV))[None]
    loss = part(11)[0]

    dmx_all = pack_all.reshape(N_DEV, -1)[:, sizes[0]:sizes[0] + sizes[1]]
    d9 = jnp.concatenate([lax.dynamic_slice(dmodc_t[None], (0, me * n_mod), (1, n_mod)),
                          lax.dynamic_slice(dmx_all, (0, me * n_mod), (N_DEV, n_mod)),
                          jnp.zeros((16 - 1 - N_DEV, n_mod), F32)], axis=0)
    g_w_mod, dcc_part = _mod_bwd(a9, d9, w_mod[0])
    (cctx_handle, handle), tok = _split_start([whole([dcc_part]), w_in_chunk(2)], "c_ctx_start", dcc_part)
    sent_w_in(2, handle)
    recv["w_ff_down"] = _tie(recv["w_ff_down"], tok, "tie_down")
    update("w_ff_down")
    res["w_mod"] = [o[None] for o in _adamw(g_w_mod[None], w_mod[0], m_w_mod[0], v_w_mod[0], "adamw_w_mod")]
    for nm in ("w_out", "w_br_hg", "w_br_gla"):
        update(nm)
    dcc_all, = _split_wait(cctx_handle, "c_ctx_wait", [res["w_ff_down"][0], res["w_mod"][0]])
    g_c_ctx = _sum_devices(dcc_all)[0]

    small = [("c_ctx", c_ctx, m_c_ctx, v_c_ctx, g_c_ctx), ("b_mod", b_mod, m_b_mod, v_b_mod, g_b_mod),
             ("norm_pre1", norm_pre1, m_norm_pre1, v_norm_pre1, g_norms[0]),
             ("norm_post1", norm_post1, m_norm_post1, v_norm_post1, g_norms[1]),
             ("norm_pre2", norm_pre2, m_norm_pre2, v_norm_pre2, g_norms[2]),
             ("norm_post2", norm_post2, m_norm_post2, v_norm_post2, g_norms[3]),
             ("hg_lb", hg_lb, m_hg_lb, v_hg_lb, g_hg_lb), ("hg_onorm", hg_onorm, m_hg_onorm, v_hg_onorm, g_hg_on),
             ("gla_w_gk", gla_w_gk, m_gla_w_gk, v_gla_w_gk, g_wgk), ("gla_b_gk", gla_b_gk, m_gla_b_gk, v_gla_b_gk, g_bgk),
             ("gla_onorm", gla_onorm, m_gla_onorm, v_gla_onorm, g_gla_on)]
    flat = lambda k: jnp.concatenate([s[k].reshape(-1) for s in small]).reshape(-1, DH)
    outs = _adamw(flat(4)[None], flat(1), flat(2), flat(3), "adamw_small")
    off = 0
    for nm, w, _, _, _ in small:
        res[nm] = [o.reshape(-1)[off:off + w.size].reshape(w.shape) for o in outs]
        off += w.size

    done = [res[nm][0] for nm in names[1:]] + [res["w_mod"][0]] + [o for nm, *_ in small for o in res[nm]]
    major = lambda a: jnp.transpose(a, (2, 0, 1))
    outs, row0 = None, 0
    for i, (first, leaves, handle) in enumerate(s for s in sent if s[0].startswith("w_in")):
        half = W_IN_GRAD_CHUNKS[i][0]
        land, = _split_wait(handle, "grads_%s_wait" % first, done, srcs=w_in_grad[half])
        w_in_grad[half] = handle["srcs"]
        rows = (row0, row0 + land.shape[1])
        outs = _adamw_rows3(_sum_windows(land, "sum_windows%d" % i), major(w_in), major(m_w_in), major(v_w_in),
                            "adamw_w_in%d" % i, rows, outs)
        row0 = rows[1]
    res["w_in"] = [jnp.transpose(o, (1, 2, 0)) for o in outs]

    order = ["c_ctx", "w_mod", "b_mod", "norm_pre1", "norm_post1", "norm_pre2", "norm_post2", "w_in", "hg_lb",
             "hg_onorm", "gla_w_gk", "gla_b_gk", "gla_onorm", "w_br_hg", "w_br_gla", "w_out", "w_ff_gate", "w_ff_up",
             "w_ff_down"]
    return (loss, grad_x, *[res[n][k] for k in range(4) for n in order])
```

```python
import functools

import jax
import jax.numpy as jnp
from jax import lax
from jax.experimental import pallas as pl
from jax.experimental.pallas import tpu as pltpu

F32 = jnp.float32
BF16 = jnp.bfloat16
HI = lax.Precision.HIGHEST

N_DEV = 8
D = 1024
CTX = 256
HW = 512
DH = 128
NH = 8
D_FF = 2816
EPS = 1e-6
GLA_NORM = 16.0
CHUNK = 64
TR = 256
NCT = CTX // TR
W_IN_COLS = 7168
MAIN0 = 0
LR0 = 4608
GW = 1152
GOFF = 32
GATE_HG0 = LR0
GATE_GLA0 = LR0 + D
LEVELS = (32, 16, 8)
EXP_CLAMP = 80.0
VMEM_LIMIT = 48 * 1024 * 1024

ADAM_LR, ADAM_B1, ADAM_B2, ADAM_EPS, ADAM_WD, ADAM_STEP = 0.001, 0.9, 0.999, 1e-08, 0.01, 10


def _cp(*sem):
    return pltpu.CompilerParams(dimension_semantics=sem, vmem_limit_bytes=VMEM_LIMIT)


def _sig(x):
    return jax.nn.sigmoid(x)


def _silu(x):
    return x * _sig(x)


def _dsilu(x):
    s = _sig(x)
    return s * (1.0 + x * (1.0 - s))


def _rstd(x):
    return lax.rsqrt(jnp.mean(x * x, axis=-1, keepdims=True) + EPS)


def _rms_bwd(a, y, r):
    return r * (a - y * (r * r) * jnp.mean(a * y, axis=-1, keepdims=True))


def _colsum(x):
    return jnp.sum(x, axis=0, keepdims=True)


def _dot(a, b, dims, precision=None):
    return lax.dot_general(a, b, (dims, ((), ())), preferred_element_type=F32, precision=precision)


NN = ((1,), (0,))
NT = ((1,), (1,))
TN = ((0,), (0,))

SCAN_HEADS_FWD = 4
SCAN_HEADS_BWD = 4


def _split_dot(m, x):
    mb = m.astype(BF16)
    x1 = x.astype(BF16)
    r1 = x - x1.astype(F32)
    x2 = r1.astype(BF16)
    x3 = (r1 - x2.astype(F32)).astype(BF16)
    return _dot(mb, x1, NN) + _dot(mb, x2, NN) + _dot(mb, x3, NN)


def _matmul(a, b, dims, out_dtype, name, tm, tn, tk, a_off=0, m_out=None):
    a_pair = isinstance(a, (tuple, list))
    as_ = list(a) if a_pair else [a]
    a = as_[0]
    pair = isinstance(b, (tuple, list))
    bs = list(b) if pair else [b]
    b1 = bs[0]
    rows = b1.shape[0] * len(bs)
    half = None
    if dims == NN:
        m, k, n = a.shape[0], rows, b1.shape[1]
        a_spec = pl.BlockSpec((tm, tk), lambda i, j, kk: (i, kk + a_off))
        half = b1.shape[0] // tk
        if a_pair:
            assert pair and a.shape[1] == b1.shape[0] and a_off == 0
            a_spec = [pl.BlockSpec((tm, tk), lambda i, j, kk: (i, jnp.minimum(kk, half - 1))),
                      pl.BlockSpec((tm, tk), lambda i, j, kk: (i, jnp.maximum(kk - half, 0)))]
        b_maps = [lambda i, j, kk: (kk, j)] if not pair else [
            lambda i, j, kk: (jnp.minimum(kk, half - 1), j), lambda i, j, kk: (jnp.maximum(kk - half, 0), j)]
        b_specs = [pl.BlockSpec((tk, tn), f) for f in b_maps]
        axis = 2
    elif dims == NT:
        m, k, n = a.shape[0], b1.shape[1], rows
        a_spec = pl.BlockSpec((tm, tk), lambda i, j, kk: (i, kk + a_off))
        half = b1.shape[0] // tn
        b_maps = [lambda i, j, kk: (j, kk)] if not pair else [
            lambda i, j, kk: (jnp.minimum(j, half - 1), kk), lambda i, j, kk: (jnp.maximum(j - half, 0), kk)]
        b_specs = [pl.BlockSpec((tn, tk), f) for f in b_maps]
        axis = 1
    else:
        assert not pair
        m, k = (a.shape[1] if m_out is None else m_out), a.shape[0]
        n = b1.shape[1]
        a_spec = pl.BlockSpec((tk, tm), lambda i, j, kk: (kk, i + a_off))
        b_specs = [pl.BlockSpec((tk, tn), lambda i, j, kk: (kk, j))]
    assert m % tm == 0 and n % tn == 0 and k % tk == 0, (name, m, n, k, tm, tn, tk)
    nk = k // tk
    nb = len(bs)
    na = len(as_)
    assert na == 1 or dims == NN

    def body(*refs):
        a_refs, refs = refs[:na], refs[na:]
        o_ref = refs[nb]
        if pair:
            bv = jnp.where(pl.program_id(axis) < half, refs[0][...], refs[1][...])
        else:
            bv = refs[0][...]
        av = a_refs[0][...] if na == 1 else jnp.where(pl.program_id(2) < half, a_refs[0][...], a_refs[1][...])
        part = _dot(av, bv, dims)
        if nk == 1:
            o_ref[...] = part.astype(o_ref.dtype)
            return
        acc_ref = refs[nb + 1]
        kk = pl.program_id(2)

        @pl.when(kk == 0)
        def _():
            acc_ref[...] = part

        @pl.when(kk > 0)
        def _():
            acc_ref[...] += part

        @pl.when(kk == nk - 1)
        def _():
            o_ref[...] = acc_ref[...].astype(o_ref.dtype)

    return pl.pallas_call(
        body,
        name=name,
        grid=(m // tm, n // tn, nk),
        in_specs=(a_spec if a_pair else [a_spec]) + b_specs,
        out_specs=pl.BlockSpec((tm, tn), lambda i, j, kk: (i, j)),
        out_shape=jax.ShapeDtypeStruct((m, n), out_dtype),
        scratch_shapes=[] if nk == 1 else [pltpu.VMEM((tm, tn), F32)],
        compiler_params=_cp("parallel", "parallel", "arbitrary"),
    )(*as_, *bs)


def _mm_gu_act(h, w_gate_t, w_up_t, name, tm):
    t = h.shape[0]
    tn = D_FF // 2

    def body(a_ref, bg_ref, bu_ref, u_ref, v_ref, act_ref):
        a = a_ref[...]
        u = _dot(a, bg_ref[...], NT)
        v = _dot(a, bu_ref[...], NT)
        u_ref[...] = u.astype(BF16)
        v_ref[...] = v.astype(BF16)
        act_ref[...] = (_silu(u) * v).astype(BF16)

    wspec = pl.BlockSpec((tn, D), lambda i, j: (j, 0))
    ospec = pl.BlockSpec((tm, tn), lambda i, j: (i, j))
    out = jax.ShapeDtypeStruct((t, D_FF), BF16)
    return pl.pallas_call(
        body, name=name, grid=(t // tm, D_FF // tn),
        in_specs=[pl.BlockSpec((tm, D), lambda i, j: (i, 0)), wspec, wspec],
        out_specs=[ospec] * 3, out_shape=[out] * 3,
        compiler_params=_cp("parallel", "parallel"),
    )(h, w_gate_t, w_up_t)


def _mm_down_dx_act(dy, w_down, u, v, name, tm):
    t = dy.shape[0]
    tn = D_FF // 2

    def body(a_ref, b_ref, u_ref, v_ref, du_ref, dv_ref):
        dact = _dot(a_ref[...], b_ref[...], NT)
        u = u_ref[...].astype(F32)
        du_ref[...] = (dact * v_ref[...].astype(F32) * _dsilu(u)).astype(BF16)
        dv_ref[...] = (dact * _silu(u)).astype(BF16)

    ospec = pl.BlockSpec((tm, tn), lambda i, j: (i, j))
    out = jax.ShapeDtypeStruct((t, D_FF), BF16)
    return pl.pallas_call(
        body, name=name, grid=(t // tm, D_FF // tn),
        in_specs=[pl.BlockSpec((tm, D), lambda i, j: (i, 0)), pl.BlockSpec((tn, D), lambda i, j: (j, 0)), ospec, ospec],
        out_specs=[ospec] * 2, out_shape=[out] * 2,
        compiler_params=_cp("parallel", "parallel"),
    )(dy, w_down, u, v)


def _row(c):
    return pl.BlockSpec((TR, c), lambda i: (i, 0))


def _rowcol(width, cb):
    return pl.BlockSpec((TR, width), lambda i: (i, cb))


def _full(shape):
    return pl.BlockSpec(shape, lambda i: (0,) * len(shape))


def _mod_row(mc_ref, mx_ref, k, is_ctx):
    return jnp.where(is_ctx, mc_ref[k:k + 1, :], mx_ref[k:k + 1, :])


def _z_specs():
    return [pl.BlockSpec((TR, D), lambda i: (jnp.minimum(i, NCT - 1), 0)),
            pl.BlockSpec((TR, D), lambda i: (jnp.maximum(i - NCT, 0), 0))]


def _z_tile(c_ref, x_ref, is_ctx):
    return jnp.where(is_ctx, c_ref[...], x_ref[...])


def _acc_row(ref, k, val):
    ref[k:k + 1, :] += val


def _acc_mod(ref, k, is_ctx, val):
    zero = jnp.zeros_like(val)
    ref[k:k + 1, :] += jnp.where(is_ctx, val, zero)
    ref[k + 1:k + 2, :] += jnp.where(is_ctx, zero, val)


def _prenorm(z, nw, modc, modx, i_shift, i_scale, name):
    t = z[0].shape[0] + z[1].shape[0]

    def body(zc_ref, zx_ref, nw_ref, mc_ref, mx_ref, h_ref):
        is_ctx = pl.program_id(0) < NCT
        x = _z_tile(zc_ref, zx_ref, is_ctx)
        n = x * _rstd(x) * nw_ref[...]
        h = n * (1.0 + _mod_row(mc_ref, mx_ref, i_scale, is_ctx)) + _mod_row(mc_ref, mx_ref, i_shift, is_ctx)
        h_ref[...] = h.astype(BF16)

    return pl.pallas_call(
        body, name=name, grid=(t // TR,),
        in_specs=_z_specs() + [_full((1, D)), _full((8, D)), _full((8, D))],
        out_specs=_row(D),
        out_shape=jax.ShapeDtypeStruct((t, D), BF16),
        compiler_params=_cp("parallel"),
    )(*z, nw, modc, modx)


def _hg_lb(lb_ref, d):
    a0 = lb_ref[0, d:d + 1, :]
    a1 = lb_ref[1, d:d + 1, :]
    mx = jnp.maximum(a0, a1)
    e0 = jnp.exp(a0 - mx)
    e1 = jnp.exp(a1 - mx)
    return e0 / (e0 + e1)


def _log_sigmoid(x):
    return jnp.minimum(x, 0.0) - jnp.log(1.0 + jnp.exp(-jnp.abs(x)))


def _gates_fwd(p, hg_lb, wgk, bgk):
    t = p.shape[0]
    seg = lambda j: _rowcol(HW, MAIN0 // HW + j)

    def body(hq_ref, hi_ref, hf_ref, hb_ref, gq_ref, gk_ref, gv_ref, lr_ref, lb_ref, wgk_ref, bgk_ref,
             q_ref, v_ref, kf_ref, kb_ref, gf_ref, gb_ref):
        q_ref[:, :HW] = _silu(hq_ref[...].astype(F32)).astype(BF16)
        q_ref[:, HW:] = (gq_ref[...].astype(F32) * (DH ** -0.5)).astype(BF16)
        v_ref[:, :HW] = hi_ref[...]
        v_ref[:, HW:] = gv_ref[...]
        xg = _dot(lr_ref[...].astype(BF16), wgk_ref[...], NN) + bgk_ref[...]
        for d, (raw_ref, k_ref, g_ref) in enumerate(((hf_ref, kf_ref, gf_ref), (hb_ref, kb_ref, gb_ref))):
            lbd = _hg_lb(lb_ref, d)
            f = lbd + (1.0 - lbd) * _sig(raw_ref[...].astype(F32))
            k_ref[:, :HW] = (1.0 - f).astype(BF16)
            k_ref[:, HW:] = gk_ref[...]
            g_ref[:, :HW] = jnp.log(f)
            g_ref[:, HW:] = _log_sigmoid(xg[:, d * HW:(d + 1) * HW]) * (1.0 / GLA_NORM)

    out = jax.ShapeDtypeStruct((t, D), F32)
    outb = jax.ShapeDtypeStruct((t, D), BF16)
    return pl.pallas_call(
        body, name="gates_fwd", grid=(t // TR,),
        in_specs=[seg(0), seg(1), seg(2), seg(3), seg(5), seg(6), seg(7), _rowcol(DH, LR0 // DH),
                  _full((2, 2, HW)), _full((DH, D)), _full((1, D))],
        out_specs=[_row(D)] * 6,
        out_shape=[outb] * 4 + [out] * 2,
        compiler_params=_cp("parallel"),
    )(p, p, p, p, p, p, p, p, hg_lb, wgk, bgk)


def _post_fwd(o_fw, o_bw, p, onw):
    t = o_fw.shape[0]

    def body(of_ref, ob_ref, g1_ref, g2_ref, w_ref, y_ref):
        for h in range(NH):
            sl = slice(h * DH, (h + 1) * DH)
            o = of_ref[:, sl] + ob_ref[:, sl]
            g_ref = g1_ref if h < NH // 2 else g2_ref
            gs = slice((h % (NH // 2)) * DH, (h % (NH // 2) + 1) * DH)
            n = o * _rstd(o) * w_ref[:, sl]
            y_ref[:, sl] = (n * _silu(g_ref[:, gs].astype(F32))).astype(BF16)

    return pl.pallas_call(
        body, name="post_fwd", grid=(t // TR,),
        in_specs=[_row(D), _row(D), _rowcol(HW, MAIN0 // HW + 4), _rowcol(HW, MAIN0 // HW + 8), _full((1, D))],
        out_specs=_row(D),
        out_shape=jax.ShapeDtypeStruct((t, D), BF16),
        compiler_params=_cp("parallel"),
    )(o_fw, o_bw, p, p, onw)


def _gate_window_specs(col0):
    return [_rowcol(HW, col0 // HW), _rowcol(HW, col0 // HW + 1), _rowcol(DH, (col0 + 2 * HW) // DH)]


def _gate_window(refs):
    return jnp.concatenate([r[...].astype(F32) for r in refs], axis=1)


def _branch_merge(y, w_hg, w_gla, p):
    t = y.shape[0]

    def body(y_ref, wh_ref, wg_ref, a0, a1, a2, b0, b1, b2, u1_ref, u2_ref, m_ref):
        u1 = _dot(y_ref[:, :HW], wh_ref[...], NN)
        u2 = _dot(y_ref[:, HW:], wg_ref[...], NN)
        u1_ref[...] = u1.astype(BF16)
        u2_ref[...] = u2.astype(BF16)
        m_ref[...] = (_sig(_gate_window((a0, a1, a2))) * u1 + _sig(_gate_window((b0, b1, b2))) * u2).astype(BF16)

    out = jax.ShapeDtypeStruct((t, GW), BF16)
    return pl.pallas_call(
        body, name="branch_merge", grid=(t // TR,),
        in_specs=[_row(D), _full((HW, GW)), _full((HW, GW))] + _gate_window_specs(GATE_HG0)
        + _gate_window_specs(GATE_GLA0),
        out_specs=[_row(GW)] * 3, out_shape=[out] * 3,
        compiler_params=_cp("parallel"),
    )(y, w_hg, w_gla, p, p, p, p, p, p)


def _mid_fwd(z, y1, nw_post, nw_pre, modc, modx):
    t = y1.shape[0]

    def body(zc_ref, zx_ref, y_ref, wpo_ref, wpr_ref, mc_ref, mx_ref, z1_ref, h_ref):
        is_ctx = pl.program_id(0) < NCT
        y = y_ref[...].astype(F32)
        z1 = _z_tile(zc_ref, zx_ref, is_ctx) + _mod_row(mc_ref, mx_ref, 2, is_ctx) * (y * _rstd(y) * wpo_ref[...])
        z1_ref[...] = z1
        n = z1 * _rstd(z1) * wpr_ref[...]
        h = n * (1.0 + _mod_row(mc_ref, mx_ref, 4, is_ctx)) + _mod_row(mc_ref, mx_ref, 3, is_ctx)
        h_ref[...] = h.astype(BF16)

    return pl.pallas_call(
        body, name="mid_fwd", grid=(t // TR,),
        in_specs=_z_specs() + [_row(D), _full((1, D)), _full((1, D)), _full((8, D)), _full((8, D))],
        out_specs=[_row(D), _row(D)],
        out_shape=[jax.ShapeDtypeStruct((t, D), F32), jax.ShapeDtypeStruct((t, D), BF16)],
        compiler_params=_cp("parallel"),
    )(*z, y1, nw_post, nw_pre, modc, modx)


def _final(z1, y2, target, nw, modc, modx):
    t = z1.shape[0]

    def body(z1_ref, y_ref, tg_ref, w_ref, mc_ref, mx_ref, dz_ref, dy_ref, loss_ref, sm_ref):
        i = pl.program_id(0)
        is_ctx = i < NCT

        @pl.when(i == 0)
        def _():
            loss_ref[...] = jnp.zeros_like(loss_ref)
            sm_ref[...] = jnp.zeros_like(sm_ref)

        g = _mod_row(mc_ref, mx_ref, 5, is_ctx)
        y = y_ref[...].astype(F32)
        r = _rstd(y)
        w = w_ref[...]
        yr = y * r
        n = yr * w
        e = z1_ref[...] + g * n - tg_ref[...]
        lat = jnp.where(is_ctx, 0.0, 1.0)
        loss_ref[...] += lat * _colsum(e * e)
        dz = e * (lat / D)
        dz_ref[...] = dz
        _acc_mod(sm_ref, 0, is_ctx, _colsum(dz * n))
        dn = dz * g
        _acc_row(sm_ref, 2, _colsum(dn * yr))
        dy_ref[...] = _rms_bwd(dn * w, y, r).astype(BF16)

    return pl.pallas_call(
        body, name="final", grid=(t // TR,),
        in_specs=[_row(D), _row(D), pl.BlockSpec((TR, D), lambda i: (jnp.maximum(i - NCT, 0), 0)),
                  _full((1, D)), _full((8, D)), _full((8, D))],
        out_specs=[_row(D), _row(D), _full((1, D)), _full((8, D))],
        out_shape=[jax.ShapeDtypeStruct((t, D), F32), jax.ShapeDtypeStruct((t, D), BF16),
                   jax.ShapeDtypeStruct((1, D), F32), jax.ShapeDtypeStruct((8, D), F32)],
        compiler_params=_cp("arbitrary"),
    )(z1, y2, target, nw, modc, modx)


def _mid_bwd(dh2, dz, z1, y1, nw_post, nw_pre, modc, modx):
    t = z1.shape[0]

    def body(dh_ref, dz_ref, z1_ref, y_ref, wpo_ref, wpr_ref, mc_ref, mx_ref, dzo_ref, dy_ref, sm_ref):
        i = pl.program_id(0)
        is_ctx = i < NCT

        @pl.when(i == 0)
        def _():
            sm_ref[...] = jnp.zeros_like(sm_ref)

        dh = dh_ref[...].astype(F32)
        z1 = z1_ref[...]
        r = _rstd(z1)
        zr = z1 * r
        wpr = wpr_ref[...]
        n = zr * wpr
        _acc_mod(sm_ref, 0, is_ctx, _colsum(dh))
        _acc_mod(sm_ref, 2, is_ctx, _colsum(dh * n))
        dn = dh * (1.0 + _mod_row(mc_ref, mx_ref, 4, is_ctx))
        _acc_row(sm_ref, 6, _colsum(dn * zr))
        dz1 = dz_ref[...] + _rms_bwd(dn * wpr, z1, r)
        dzo_ref[...] = dz1
        y = y_ref[...].astype(F32)
        r1 = _rstd(y)
        yr = y * r1
        wpo = wpo_ref[...]
        g = _mod_row(mc_ref, mx_ref, 2, is_ctx)
        _acc_mod(sm_ref, 4, is_ctx, _colsum(dz1 * (yr * wpo)))
        dn1 = dz1 * g
        _acc_row(sm_ref, 7, _colsum(dn1 * yr))
        dy_ref[...] = _rms_bwd(dn1 * wpo, y, r1).astype(BF16)

    return pl.pallas_call(
        body, name="mid_bwd", grid=(t // TR,),
        in_specs=[_row(D)] * 4 + [_full((1, D)), _full((1, D)), _full((8, D)), _full((8, D))],
        out_specs=[_row(D), _row(D), _full((8, D))],
        out_shape=[jax.ShapeDtypeStruct((t, D), F32), jax.ShapeDtypeStruct((t, D), BF16),
                   jax.ShapeDtypeStruct((8, D), F32)],
        compiler_params=_cp("arbitrary"),
    )(dh2, dz, z1, y1, nw_post, nw_pre, modc, modx)


def _pre_bwd(dh1, dz, z, nw, modc, modx):
    t = dh1.shape[0]

    def body(dh_ref, dz_ref, zc_ref, zx_ref, w_ref, mc_ref, mx_ref, dzo_ref, sm_ref):
        i = pl.program_id(0)
        is_ctx = i < NCT

        @pl.when(i == 0)
        def _():
            sm_ref[...] = jnp.zeros_like(sm_ref)

        dh = dh_ref[...].astype(F32)
        x = _z_tile(zc_ref, zx_ref, is_ctx)
        r = _rstd(x)
        xr = x * r
        w = w_ref[...]
        _acc_mod(sm_ref, 0, is_ctx, _colsum(dh))
        _acc_mod(sm_ref, 2, is_ctx, _colsum(dh * (xr * w)))
        dn = dh * (1.0 + _mod_row(mc_ref, mx_ref, 1, is_ctx))
        _acc_row(sm_ref, 4, _colsum(dn * xr))
        dzo_ref[...] = dz_ref[...] + _rms_bwd(dn * w, x, r)

    return pl.pallas_call(
        body, name="pre_bwd", grid=(t // TR,),
        in_specs=[_row(D)] * 2 + _z_specs() + [_full((1, D)), _full((8, D)), _full((8, D))],
        out_specs=[pl.BlockSpec((TR, D), lambda i: (jnp.maximum(i - NCT, 0), 0)), _full((8, D))],
        out_shape=[jax.ShapeDtypeStruct((t - CTX, D), F32), jax.ShapeDtypeStruct((8, D), F32)],
        compiler_params=_cp("arbitrary"),
    )(dh1, dz, *z, nw, modc, modx)


def _branch_merge_bwd(dm, p, u1, u2, w_hg, w_gla):
    t = dm.shape[0]

    def body(dm_ref, a0, a1, a2, b0, b1, b2, u1_ref, u2_ref, wh_ref, wg_ref, du1_ref, du2_ref, dg_ref, dyh_ref, dyg_ref):
        dm_ = dm_ref[...].astype(F32)
        s1 = _sig(_gate_window((a0, a1, a2)))
        s2 = _sig(_gate_window((b0, b1, b2)))
        du1 = (dm_ * s1).astype(BF16)
        du2 = (dm_ * s2).astype(BF16)
        du1_ref[...] = du1
        du2_ref[...] = du2
        dg_ref[:, :GW] = (dm_ * u1_ref[...].astype(F32) * s1 * (1.0 - s1)).astype(BF16)
        dg_ref[:, GW:] = (dm_ * u2_ref[...].astype(F32) * s2 * (1.0 - s2)).astype(BF16)
        dyh_ref[...] = _dot(du1, wh_ref[...], NT).astype(BF16)
        dyg_ref[...] = _dot(du2, wg_ref[...], NT).astype(BF16)

    return pl.pallas_call(
        body, name="branch_merge_bwd", grid=(t // TR,),
        in_specs=[_row(GW)] + _gate_window_specs(GATE_HG0) + _gate_window_specs(GATE_GLA0)
        + [_row(GW), _row(GW), _full((HW, GW)), _full((HW, GW))],
        out_specs=[_row(GW), _row(GW), _row(2 * GW), _row(HW), _row(HW)],
        out_shape=[jax.ShapeDtypeStruct((t, GW), BF16), jax.ShapeDtypeStruct((t, GW), BF16),
                   jax.ShapeDtypeStruct((t, 2 * GW), BF16), jax.ShapeDtypeStruct((t, HW), BF16),
                   jax.ShapeDtypeStruct((t, HW), BF16)],
        compiler_params=_cp("parallel"),
    )(dm, p, p, p, p, p, p, u1, u2, w_hg, w_gla)


def _post_bwd(dy_hg, dy_gla, o_fw, o_bw, p, onw):
    t = o_fw.shape[0]

    def body(d1_ref, d2_ref, of_ref, ob_ref, g1_ref, g2_ref, w_ref, do_ref, dg_ref, sm_ref):
        @pl.when(pl.program_id(0) == 0)
        def _():
            sm_ref[...] = jnp.zeros_like(sm_ref)

        for h in range(NH):
            sl = slice(h * DH, (h + 1) * DH)
            gs = slice((h % (NH // 2)) * DH, (h % (NH // 2) + 1) * DH)
            g_ref, d_ref = (g1_ref, d1_ref) if h < NH // 2 else (g2_ref, d2_ref)
            o = of_ref[:, sl] + ob_ref[:, sl]
            r = _rstd(o)
            orr = o * r
            w = w_ref[:, sl]
            gt = g_ref[:, gs].astype(F32)
            dy = d_ref[:, gs].astype(F32)
            dg_ref[:, sl] = (dy * (orr * w) * _dsilu(gt)).astype(BF16)
            dn = dy * _silu(gt)
            sm_ref[0:1, sl] += _colsum(dn * orr)
            do_ref[:, sl] = _rms_bwd(dn * w, o, r)

    return pl.pallas_call(
        body, name="post_bwd", grid=(t // TR,),
        in_specs=[_row(HW), _row(HW), _row(D), _row(D), _rowcol(HW, MAIN0 // HW + 4), _rowcol(HW, MAIN0 // HW + 8),
                  _full((1, D))],
        out_specs=[_row(D), _row(D), _full((8, D))],
        out_shape=[jax.ShapeDtypeStruct((t, D), F32), jax.ShapeDtypeStruct((t, D), BF16),
                   jax.ShapeDtypeStruct((8, D), F32)],
        compiler_params=_cp("arbitrary"),
    )(dy_hg, dy_gla, o_fw, o_bw, p, p, onw)


def _gates_bwd(p, hg_lb, wgk, bgk, dgm, dgo, dq_f, dq_b, dv_f, dv_b, dk_f, dk_b, dg_f, dg_b):
    t = p.shape[0]
    seg = lambda j: _rowcol(HW, MAIN0 // HW + j)

    def body(hq_ref, hf_ref, hb_ref, lr_ref, lb_ref, wgk_ref, bgk_ref, dgm_ref, dgo_ref,
             dqf_ref, dqb_ref, dvf_ref, dvb_ref, dkf_ref, dkb_ref, dgf_ref, dgb_ref,
             dp_ref, dlb_ref, dw_ref, db_ref):
        @pl.when(pl.program_id(0) == 0)
        def _():
            dlb_ref[...] = jnp.zeros_like(dlb_ref)
            dw_ref[...] = jnp.zeros_like(dw_ref)
            db_ref[...] = jnp.zeros_like(db_ref)

        c0 = MAIN0

        def put(j, val):
            dp_ref[:, c0 + j * HW:c0 + (j + 1) * HW] = val.astype(BF16)

        dq = dqf_ref[...].astype(F32) + dqb_ref[...].astype(F32)
        dv = dvf_ref[...].astype(F32) + dvb_ref[...].astype(F32)
        put(0, dq[:, :HW] * _dsilu(hq_ref[...].astype(F32)))
        put(1, dv[:, :HW])
        put(5, dq[:, HW:] * (DH ** -0.5))
        put(7, dv[:, HW:])
        put(6, dkf_ref[:, HW:].astype(F32) + dkb_ref[:, HW:].astype(F32))
        dp_ref[:, c0 + 4 * HW:c0 + 5 * HW] = dgo_ref[:, :HW]
        dp_ref[:, c0 + 8 * HW:c0 + 9 * HW] = dgo_ref[:, HW:]
        lr = lr_ref[...].astype(BF16)
        xg = _dot(lr, wgk_ref[...], NN) + bgk_ref[...]
        dxg = []
        for d, (raw_ref, dk_ref, dg_ref) in enumerate(((hf_ref, dkf_ref, dgf_ref), (hb_ref, dkb_ref, dgb_ref))):
            lbd = _hg_lb(lb_ref, d)
            s = _sig(raw_ref[...].astype(F32))
            f = lbd + (1.0 - lbd) * s
            df = dg_ref[:, :HW] / f - dk_ref[:, :HW].astype(F32)
            put(2 + d, df * (1.0 - lbd) * s * (1.0 - s))
            dlb_ref[d:d + 1, :] += _colsum(df * (1.0 - s)) * (lbd * (1.0 - lbd))
            dxg.append(dg_ref[:, HW:] * (1.0 / GLA_NORM) * _sig(-xg[:, d * HW:(d + 1) * HW]))
        dxg = jnp.concatenate(dxg, axis=1)
        db_ref[0:1, :] += _colsum(dxg)
        dxg_b = dxg.astype(BF16)
        dw_ref[...] += _dot(lr, dxg_b, TN)
        dlr = _dot(dxg_b, wgk_ref[...], NT)
        dp_ref[:, LR0:LR0 + DH] = (dlr + dgm_ref[:, :DH].astype(F32)).astype(BF16)
        dp_ref[:, LR0 + DH:GATE_GLA0] = dgm_ref[:, DH:D]
        dp_ref[:, GATE_GLA0:GATE_GLA0 + DH] = dgm_ref[:, D:GW] + dgm_ref[:, GW:GW + DH]
        dp_ref[:, GATE_GLA0 + DH:GATE_GLA0 + GW] = dgm_ref[:, GW + DH:]
        dp_ref[:, GATE_GLA0 + GW:] = jnp.zeros((TR, W_IN_COLS - GATE_GLA0 - GW), BF16)

    return pl.pallas_call(
        body, name="gates_bwd", grid=(t // TR,),
        in_specs=[seg(0), seg(2), seg(3), _rowcol(DH, LR0 // DH), _full((2, 2, HW)), _full((DH, D)), _full((1, D)),
                  _row(2 * GW), _row(D)] + [_row(D)] * 8,
        out_specs=[_row(W_IN_COLS), _full((8, HW)), _full((DH, D)), _full((8, D))],
        out_shape=[jax.ShapeDtypeStruct((t, W_IN_COLS), BF16), jax.ShapeDtypeStruct((8, HW), F32),
                   jax.ShapeDtypeStruct((DH, D), F32), jax.ShapeDtypeStruct((8, D), F32)],
        compiler_params=_cp("arbitrary"),
    )(p, p, p, p, hg_lb, wgk, bgk, dgm, dgo, dq_f, dq_b, dv_f, dv_b, dk_f, dk_b, dg_f, dg_b)


def _scan_consts(rev):
    r = lax.broadcasted_iota(jnp.int32, (CHUNK, CHUNK), 0)
    u = lax.broadcasted_iota(jnp.int32, (CHUNK, CHUNK), 1)
    rp = lax.broadcasted_iota(jnp.int32, (CHUNK, 1), 0)
    if rev:
        r, u, rp = CHUNK - 1 - r, CHUNK - 1 - u, CHUNK - 1 - rp
    tri = jnp.where(u <= r, 1.0, 0.0).astype(F32)
    tri_t = jnp.where(r <= u, 1.0, 0.0).astype(F32)
    lv = []
    for b in LEVELS:
        sh = b.bit_length() - 1
        pair = ((r >> sh) == (u >> sh) + 1) & (((u >> sh) & 1) == 0)
        pair_t = ((u >> sh) == (r >> sh) + 1) & (((r >> sh) & 1) == 0)
        tside = ((rp >> sh) & 1) == 1
        lv.append((pair, pair_t, tside, jnp.where(tside, 1.0, -1.0).astype(F32)))
    bd = LEVELS[-1].bit_length() - 1
    diag = ((r >> bd) == (u >> bd)) & (u <= r)
    diag_t = ((r >> bd) == (u >> bd)) & (r <= u)
    return tri, tri_t, lv, diag, diag_t


def _row_of(pos, rev):
    return CHUNK - 1 - pos if rev else pos


def _chunk_terms(cum, b_scr, consts, rev):
    _, _, lv, _, _ = consts
    terms = []
    for b, (_, _, _, sgn) in zip(LEVELS, lv):
        pieces = []
        for j in range(CHUNK // (2 * b)):
            row = _row_of(2 * b * j + b - 1, rev)
            pieces.append(jnp.broadcast_to(b_scr[row:row + 1, :], (2 * b, DH)))
        if rev:
            pieces = pieces[::-1]
        bnd = pieces[0] if len(pieces) == 1 else jnp.concatenate(pieces, axis=0)
        terms.append(jnp.exp((cum - bnd) * sgn))
    b = LEVELS[-1]
    pieces = []
    for j in range(CHUNK // b):
        if j == 0:
            pieces.append(jnp.zeros((b, DH), F32))
        else:
            row = _row_of(b * j - 1, rev)
            pieces.append(jnp.broadcast_to(b_scr[row:row + 1, :], (b, DH)))
    if rev:
        pieces = pieces[::-1]
    start = jnp.concatenate(pieces, axis=0)
    wq = jnp.exp(jnp.minimum(cum - start, 0.0))
    wk = jnp.exp(jnp.minimum(start - cum, EXP_CLAMP))
    terms.append((wq, wk))
    return terms


def _run_staged(units):
    live = list(units)
    while live:
        nxt = []
        for u in live:
            try:
                next(u)
                nxt.append(u)
            except StopIteration:
                pass
        live = nxt


SCAN_TB = 256
SCAN_CB = SCAN_TB // CHUNK


def _block_order(i, ntb, rev):
    nctx = CTX // SCAN_TB
    if not rev:
        return i
    return jnp.where(i < nctx, nctx - 1 - i, ntb - 1 - (i - nctx))


def _chunk_in_block(j, rev):
    return SCAN_CB - 1 - j if rev else j


def _scan_fwd(q, k, v, g, rev):
    t = q.shape[0]
    nc = t // CHUNK
    hpb = SCAN_HEADS_FWD

    def body(q_ref, k_ref, v_ref, g_ref, o_ref, st_ref, s_scr, b_scr):
        consts = _scan_consts(rev)
        _, _, lv, diag, _ = consts
        masks = [lvl[0] for lvl in lv] + [diag]

        @pl.when(pl.program_id(1) == 0)
        def _():
            s_scr[...] = jnp.zeros_like(s_scr)

        tri = consts[0]
        state = {hh: s_scr[hh] for hh in range(hpb)}

        def unit(hh, j):
            sl = slice(hh * DH, (hh + 1) * DH)
            c = _chunk_in_block(j, rev)
            rows = slice(c * CHUNK, (c + 1) * CHUNK)
            b_ref = b_scr.at[hh * SCAN_CB + j]
            qc, kc, vc, gc = q_ref[rows, sl], k_ref[rows, sl], v_ref[rows, sl], g_ref[rows, sl]
            cum = _split_dot(tri, gc)
            b_ref[...] = cum
            yield
            terms = _chunk_terms(cum, b_ref, consts, rev)
            qf, kf = qc.astype(F32), kc.astype(F32)
            xs = [(jnp.where(tside, qf, kf) * w).astype(BF16) for w, (_, _, tside, _) in zip(terms[:-1], lv)]
            qd, kd = (qf * terms[-1][0]).astype(BF16), (kf * terms[-1][1]).astype(BF16)
            tot = _colsum(gc)
            qe = (qf * jnp.exp(cum)).astype(BF16)
            ke = (kf * jnp.exp(tot - cum)).astype(BF16)
            vb = vc.astype(BF16)
            yield
            scs = [_dot(x, x, NT) for x in xs] + [_dot(qd, kd, NT)]
            kv = _dot(vb, ke, TN)
            yield
            a = jnp.zeros((CHUNK, CHUNK), F32)
            for sc, m in zip(scs, masks):
                a = a + jnp.where(m, sc, 0.0)
            o_intra = _dot(a.astype(BF16), vb, NN)
            yield
            st = state[hh]
            st_ref[hh, c] = st
            o_ref[rows, sl] = o_intra + _dot(qe, st.astype(BF16), NT)
            state[hh] = st * jnp.exp(tot) + kv
            yield

        _run_staged([unit(hh, j) for hh in range(hpb) for j in range(SCAN_CB)])
        for hh in range(hpb):
            s_scr[hh] = state[hh]

    ntb = t // SCAN_TB
    col = pl.BlockSpec((SCAN_TB, hpb * DH), lambda h, i: (_block_order(i, ntb, rev), h))
    return pl.pallas_call(
        body, name="scan_fwd_" + ("bw" if rev else "fw"), grid=(NH // hpb, ntb),
        in_specs=[col] * 4,
        out_specs=[col, pl.BlockSpec((hpb, SCAN_CB, DH, DH), lambda h, i: (h, _block_order(i, ntb, rev), 0, 0))],
        out_shape=[jax.ShapeDtypeStruct((t, D), F32), jax.ShapeDtypeStruct((NH, nc, DH, DH), F32)],
        scratch_shapes=[pltpu.VMEM((hpb, DH, DH), F32), pltpu.VMEM((hpb * SCAN_CB, CHUNK, DH), F32)],
        compiler_params=_cp("parallel", "arbitrary"),
    )(q, k, v, g)


def _scan_bwd(q, k, v, g, do, states, rev):
    t = q.shape[0]
    nc = t // CHUNK
    hpb = SCAN_HEADS_BWD

    def body(q_ref, k_ref, v_ref, g_ref, do_ref, st_ref, dq_ref, dk_ref, dv_ref, dg_ref, ds_scr, b_scr):
        consts = _scan_consts(rev)
        _, tri_t, lv, diag, diag_t = consts
        masks = [(lvl[0], lvl[1]) for lvl in lv] + [(diag, diag_t)]
        @pl.when(pl.program_id(1) == 0)
        def _():
            ds_scr[...] = jnp.zeros_like(ds_scr)

        tri = consts[0]
        dstate = {hh: ds_scr[hh] for hh in range(hpb)}

        def unit(hh, jj):
            sl = slice(hh * DH, (hh + 1) * DH)
            c = _chunk_in_block(SCAN_CB - 1 - jj, rev)
            rows = slice(c * CHUNK, (c + 1) * CHUNK)
            b_ref = b_scr.at[hh * SCAN_CB + jj]
            qc, kc, vc, gc = q_ref[rows, sl], k_ref[rows, sl], v_ref[rows, sl], g_ref[rows, sl]
            dob = do_ref[rows, sl].astype(BF16)
            vb = vc.astype(BF16)
            cum = _split_dot(tri, gc)
            b_ref[...] = cum
            da = _dot(dob, vb, NT)
            da_t = _dot(vb, dob, NT)
            yield
            terms = _chunk_terms(cum, b_ref, consts, rev)
            qf, kf = qc.astype(F32), kc.astype(F32)
            xs = [(jnp.where(tside, qf, kf) * w).astype(BF16) for w, (_, _, tside, _) in zip(terms[:-1], lv)]
            wqd, wkd = terms[-1]
            qdb, kdb = (qf * wqd).astype(BF16), (kf * wkd).astype(BF16)
            tot = _colsum(gc)
            e_tot = jnp.exp(tot)
            e_b = jnp.exp(cum)
            e_t = jnp.exp(tot - cum)
            qeb = (qf * e_b).astype(BF16)
            keb = (kf * e_t).astype(BF16)
            dsym = [(jnp.where(m, da, 0.0) + jnp.where(m_t, da_t, 0.0)).astype(BF16) for m, m_t in masks[:-1]]
            dad = (jnp.where(diag, da, 0.0).astype(BF16), jnp.where(diag_t, da_t, 0.0).astype(BF16))
            yield
            sym = [_dot(x, x, NT) for x in xs]
            dxs = [_dot(d, x, NN) for d, x in zip(dsym, xs)]
            at_d = _dot(kdb, qdb, NT)
            dqt_d = _dot(dad[0], kdb, NN)
            dkt_d = _dot(dad[1], qdb, NN)
            qd = _dot(dob, qeb, TN)
            yield
            a_t = jnp.where(diag_t, at_d, 0.0)
            dq = dqt_d * wqd
            dk = dkt_d * wkd
            db = dqt_d * qdb.astype(F32) - dkt_d * kdb.astype(F32)
            for s, dx, x, w, (_, m_t, tside, sgn) in zip(sym, dxs, xs, terms[:-1], lv):
                a_t = a_t + jnp.where(m_t, s, 0.0)
                dxw = dx * w
                dq = dq + jnp.where(tside, dxw, 0.0)
                dk = dk + jnp.where(tside, 0.0, dxw)
                db = db + (dx * x.astype(F32)) * sgn
            dv_intra = _dot(a_t.astype(BF16), dob, NN)
            st = st_ref[hh, c]
            stb = st.astype(BF16)
            dqe = _dot(dob, stb, NN)
            yield
            dst = dstate[hh]
            dstb = dst.astype(BF16)
            dstate[hh] = dst * e_tot + qd
            dv_ref[rows, sl] = (dv_intra + _dot(keb, dstb, NT)).astype(BF16)
            dke = _dot(vb, dstb, NN)
            yield
            qe = qeb.astype(F32)
            ke = keb.astype(F32)
            dq_ref[rows, sl] = (dq + dqe * e_b).astype(BF16)
            dk_ref[rows, sl] = (dk + dke * e_t).astype(BF16)
            db = db + dqe * qe - dke * ke
            dtot = _colsum(dstb.astype(F32) * stb.astype(F32)) * e_tot + _colsum(dke * ke)
            dg_ref[rows, sl] = _split_dot(tri_t, db) + dtot
            yield

        _run_staged([unit(hh, jj) for hh in range(hpb) for jj in range(SCAN_CB)])
        for hh in range(hpb):
            ds_scr[hh] = dstate[hh]

    ntb = t // SCAN_TB
    blk = lambda i: _block_order(ntb - 1 - i, ntb, rev)
    col = pl.BlockSpec((SCAN_TB, hpb * DH), lambda h, i: (blk(i), h))
    out = jax.ShapeDtypeStruct((t, D), F32)
    outb = jax.ShapeDtypeStruct((t, D), BF16)
    return pl.pallas_call(
        body, name="scan_bwd_" + ("bw" if rev else "fw"), grid=(NH // hpb, ntb),
        in_specs=[col] * 5 + [pl.BlockSpec((hpb, SCAN_CB, DH, DH), lambda h, i: (h, blk(i), 0, 0))],
        out_specs=[col] * 4,
        out_shape=[outb] * 3 + [out],
        scratch_shapes=[pltpu.VMEM((hpb, DH, DH), F32), pltpu.VMEM((hpb * SCAN_CB, CHUNK, DH), F32)],
        compiler_params=_cp("parallel", "arbitrary"),
    )(q, k, v, g, do, states)


W_IN_GRAD_CHUNKS = (("a", (0, 512)), ("b", (0, 256)), ("b", (256, 512)))
W_IN_REF = 6688
W_IN_PAD = 896
W_IN_PIECE = 256
W_IN_STAGES = (3, 4)


def _assemble_w_in(g, rows, prev, name):
    n, r, wp = g.shape
    tr = W_IN_PIECE
    tiles = wp // DH
    first = rows[0] // tr

    def body(g_ref, *refs):
        o_ref = refs[-1]
        lane = lax.broadcasted_iota(jnp.int32, (tr, DH), 1)
        for t in range(W_IN_COLS // DH):
            acc = None
            for j in range(n):
                c = DH * t - W_IN_SHARD * j
                if c <= -DH or c >= W_IN_SHARD:
                    continue
                k, s = divmod(c, DH)
                lo = g_ref[j, :, k * DH:(k + 1) * DH] if 0 <= k < tiles else None
                hi = g_ref[j, :, (k + 1) * DH:(k + 2) * DH] if s and 0 <= k + 1 < tiles else None
                if s:
                    zero = jnp.zeros((tr, DH), g.dtype)
                    lo = zero if lo is None else pltpu.roll(lo, DH - s, 1)
                    hi = zero if hi is None else pltpu.roll(hi, DH - s, 1)
                    part = jnp.where(lane < DH - s, lo, hi)
                else:
                    part = lo
                acc = part if acc is None else acc + part
            o_ref[:, t * DH:(t + 1) * DH] = jnp.zeros((tr, DH), g.dtype) if acc is None else acc

    held = [] if prev is None else [prev]
    return pl.pallas_call(
        body, name=name, grid=((rows[1] - rows[0]) // tr,),
        in_specs=[pl.BlockSpec((n, tr, wp), lambda i: (0, first + i, 0))] + [pl.BlockSpec(memory_space=pl.ANY)] * len(held),
        out_specs=pl.BlockSpec((tr, W_IN_COLS), lambda i: (first + i, 0)),
        out_shape=jax.ShapeDtypeStruct((r, W_IN_COLS), g.dtype),
        input_output_aliases={1: 0} if held else {},
        compiler_params=_cp("parallel"),
    )(g, *held)


def _gate_cols(w):
    return jnp.pad(w, ((0, 0), (GOFF, GW - GOFF - D)))


def _gate_rows(w):
    return jnp.pad(w, ((GOFF, GW - GOFF - D), (0, 0)))


def _layout_wgk(w):
    r = w.shape[1]
    top = jnp.concatenate([w[0], jnp.zeros_like(w[0])], axis=1)
    bot = jnp.concatenate([jnp.zeros_like(w[1]), w[1]], axis=1)
    return jnp.concatenate([top, bot, jnp.zeros((DH - 2 * r, D), w.dtype)], axis=0)


def _unlayout_wgk(d, r=16):
    return jnp.stack([d[:r, :HW], d[r:2 * r, HW:]])


def _local_step(z, target, modc, modx, norms, onw, hg_lb, wgk, bgk, get_w_in, get_mix, get_ffn, send):
    n_pre1, n_post1, n_pre2, n_post2 = norms
    t = z[0].shape[0] + z[1].shape[0]
    tm = 1152 if t % 1152 == 0 else 256
    h1 = _prenorm(z, n_pre1, modc, modx, 0, 1, "prenorm1")
    w_in = get_w_in(h1)
    p = _matmul(h1, w_in, NN, BF16, "mm_in", t, 1024, D)
    q, v, k_f, k_b, g_f, g_b = _gates_fwd(p, hg_lb, wgk, bgk)
    o_f, st_f = _scan_fwd(q, k_f, v, g_f, False)
    o_b, st_b = _scan_fwd(q, k_b, v, g_b, True)
    y = _post_fwd(o_f, o_b, p, onw)
    w_br_hg, w_br_gla, w_out = get_mix(y)
    u1, u2, merged = _branch_merge(y, w_br_hg, w_br_gla, p)
    y1 = _matmul(merged, w_out, NN, BF16, "mm_out", tm, 512, GW)
    z1, h2 = _mid_fwd(z, y1, n_post1, n_pre2, modc, modx)
    w_gu_t, w_down = get_ffn(h2)
    u, v_ff, act = _mm_gu_act(h2, w_gu_t[0], w_gu_t[1], "mm_gu", tm)
    y2 = _matmul(act, w_down, NN, BF16, "mm_down", t, 512, D_FF)
    dz, dy2, loss_vec, sm_final = _final(z1, y2, target, n_post2, modc, modx)
    du, dv_ff = _mm_down_dx_act(dy2, w_down, u, v_ff, "mm_down_dx", tm)
    d_w_down = _matmul(act, dy2, TN, BF16, "mm_down_dw", D_FF // 2, 1024, t)
    dh2 = _matmul((du, dv_ff), w_gu_t, NN, BF16, "mm_gu_dx", tm, 512, D_FF)
    d_w_gate_t = _matmul(du, h2, TN, BF16, "mm_gate_dw", D_FF // 2, 1024, t)
    d_w_up_t = _matmul(dv_ff, h2, TN, BF16, "mm_up_dw", D_FF // 2, 1024, t)
    dh2 = send(("w_down", "w_gate_t", "w_up_t"), (d_w_down, d_w_gate_t, d_w_up_t), dh2)
    dz, dy1, sm_mid = _mid_bwd(dh2, dz, z1, y1, n_post1, n_pre2, modc, modx)
    dmerged = _matmul(dy1, w_out, NT, BF16, "mm_out_dx", tm, GW, D)
    d_w_out = _matmul(merged, dy1, TN, BF16, "mm_out_dw", GW, 512, t)
    du1, du2, dgm, dy_hg, dy_gla = _branch_merge_bwd(dmerged, p, u1, u2, w_br_hg, w_br_gla)
    d_w_br_hg = _matmul(y, du1, TN, BF16, "mm_br_hg_dw", HW, GW, t, a_off=0, m_out=HW)
    d_w_br_gla = _matmul(y, du2, TN, BF16, "mm_br_gla_dw", HW, GW, t, a_off=1, m_out=HW)
    dy_hg = send(("w_out", "w_br_hg", "w_br_gla"), (d_w_out, d_w_br_hg, d_w_br_gla), dy_hg)
    do, dgo, sm_post = _post_bwd(dy_hg, dy_gla, o_f, o_b, p, onw)
    dq_f, dk_f, dv_f, dg_f = _scan_bwd(q, k_f, v, g_f, do, st_f, False)
    dq_b, dk_b, dv_b, dg_b = _scan_bwd(q, k_b, v, g_b, do, st_b, True)
    dp, d_lb, d_wgk, d_bgk = _gates_bwd(p, hg_lb, wgk, bgk, dgm, dgo, dq_f, dq_b, dv_f, dv_b, dk_f, dk_b, dg_f, dg_b)
    d_w_in_a = _matmul(h1, dp, TN, BF16, "mm_in_dw_a", 512, 1024, t, a_off=0, m_out=D // 2)
    dp = send(("w_in_a",), (d_w_in_a,), dp)
    d_w_in_b = _matmul(h1, dp, TN, BF16, "mm_in_dw_b", 512, 1024, t, a_off=1, m_out=D // 2)
    dp = send(("w_in_b",), (d_w_in_b,), dp)
    dh1 = _matmul(dp, w_in, NT, BF16, "mm_in_dx", tm, 512, W_IN_COLS // 2)
    grad_x, sm_pre = _pre_bwd(dh1, dz, z, n_pre1, modc, modx)
    return dict(loss_vec=loss_vec, grad_x=grad_x, sm_final=sm_final, sm_mid=sm_mid, sm_post=sm_post, sm_pre=sm_pre,
                d_lb=d_lb, d_wgk=d_wgk, d_bgk=d_bgk)


MESH = pl.DeviceIdType.MESH
ANY = pl.BlockSpec(memory_space=pl.ANY)
N_REL = N_DEV - 1


def _place():
    return lax.axis_index("x"), lax.axis_index("y"), lax.axis_index("c")


def _slot(p):
    return 4 * p[0] + 2 * p[1] + p[2]


HBM = pl.BlockSpec(memory_space=pltpu.HBM)
SEM = pl.BlockSpec(memory_space=pltpu.SEMAPHORE)
EFFECT = pltpu.SideEffectType.DATAFLOW_SIDE_EFFECTING


def _peer_of(x, y, c, k):
    flip = lambda v, bit: 1 - v if bit else v
    return flip(x, k & 4), flip(y, k & 2), flip(c, k & 1)


def _view_whole(src, slot):
    return src


def _view_near(src, slot):
    return src


_view_near.peers = (1, 2, 4, 6)


def _view_near_rows(rows):
    def view(src, slot):
        return src.at[pl.ds(rows[0], rows[1] - rows[0])]
    view.peers = _view_near.peers
    view.land = lambda land, slot: land.at[slot, pl.ds(rows[0], rows[1] - rows[0])]
    return view


def _view_block(src, slot):
    return src.at[slot]


W_IN_SHARD = W_IN_REF // N_DEV


def _view_window(rows):
    def view(src, slot):
        col0 = pl.multiple_of((W_IN_SHARD * slot // DH) * DH, DH)
        return src.at[pl.ds(rows[0], rows[1] - rows[0]), pl.ds(col0, D)]
    return view


def _split_copies(view, srcs, lands, send_sems, recv_sems, local_sems):
    x, y, c = _place()
    me = _slot((x, y, c))
    into = getattr(view, "land", lambda land, slot: land.at[slot])
    local, sends, waits = [], [], []
    for a, (src, land) in enumerate(zip(srcs, lands)):
        local.append(pltpu.make_async_copy(view(src, me), into(land, me), local_sems.at[a]))
        for k in getattr(view, "peers", range(1, N_DEV)):
            peer = _peer_of(x, y, c, k)
            mine = view(src, _slot(peer))
            sems = dict(send_sem=send_sems.at[N_REL * a + k - 1], recv_sem=recv_sems.at[N_REL * a + k - 1],
                        device_id=peer, device_id_type=MESH)
            sends.append(pltpu.make_async_remote_copy(src_ref=mine, dst_ref=into(land, me), **sems))
            waits.append(pltpu.make_async_remote_copy(src_ref=mine, dst_ref=into(land, _slot(peer)), **sems))
    return local, sends, waits


def _split_start(groups, name, after):
    built = []
    for view, srcs, lands in groups:
        lands = [lax.empty(l, s.dtype) if isinstance(l, tuple) else l for l, s in zip(lands, srcs)]
        built.append((view, list(srcs), lands))
    bufs = [b for _, srcs, lands in built for b in srcs + lands]
    nb, ng = len(bufs), len(built)

    def body(*refs):
        buf_refs, sem_refs, token = refs[:nb], refs[nb + 1:nb + 1 + 3 * ng], refs[-1]
        pos = 0
        for i, (view, srcs, _) in enumerate(built):
            n = len(srcs)
            local, sends, _ = _split_copies(view, buf_refs[pos:pos + n], buf_refs[pos + n:pos + 2 * n],
                                            *sem_refs[3 * i:3 * i + 3])
            pos += 2 * n
            for cp in local + sends:
                cp.start()
        token[...] = jnp.zeros_like(token)

    sems = []
    for _, srcs, _ in built:
        n = len(srcs)
        sems += [pltpu.SemaphoreType.DMA((N_REL * n,)), pltpu.SemaphoreType.DMA((N_REL * n,)),
                 pltpu.SemaphoreType.DMA((n,))]
    hbm = lambda a: pltpu.with_memory_space_constraint(a, pltpu.HBM)
    out = pl.pallas_call(
        body, name=name,
        out_shape=(*sems, *[pltpu.HBM(b.shape, b.dtype) for b in bufs], jax.ShapeDtypeStruct((8, DH), F32)),
        in_specs=[HBM] * nb + [ANY],
        out_specs=(*([SEM] * (3 * ng)), *([HBM] * nb), pl.BlockSpec(memory_space=pltpu.VMEM)),
        input_output_aliases={i: 3 * ng + i for i in range(nb)},
        compiler_params=pltpu.CompilerParams(has_side_effects=EFFECT),
    )(*[hbm(b) for b in bufs], after)
    handles, pos = [], 3 * ng
    for i, (view, srcs, _) in enumerate(built):
        n = len(srcs)
        handles.append(dict(view=view, n=n, sems=out[3 * i:3 * i + 3], srcs=list(out[pos:pos + n]),
                            lands=list(out[pos + n:pos + 2 * n])))
        pos += 2 * n
    return handles, out[-1]


def _split_wait(handle, name, after, srcs=None, lands=None):
    view, n, sems = handle["view"], handle["n"], handle["sems"]
    srcs = handle["srcs"] if srcs is None else srcs
    lands = handle["lands"] if lands is None else lands
    afters = list(after) if isinstance(after, (list, tuple)) else [after]

    def body(*refs):
        src_refs, land_refs = refs[:n], refs[n:2 * n]
        send_sems, recv_sems, local_sems = refs[2 * n:2 * n + 3]
        local, _, waits = _split_copies(view, src_refs, land_refs, send_sems, recv_sems, local_sems)
        for cp in waits:
            cp.wait_send()
            cp.wait_recv()
        for cp in local:
            cp.wait()

    out = pl.pallas_call(
        body, name=name,
        out_shape=(*[pltpu.HBM(s.shape, s.dtype) for s in srcs], *[pltpu.HBM(l.shape, l.dtype) for l in lands]),
        in_specs=[HBM] * (2 * n) + [SEM, SEM, SEM] + [ANY] * len(afters),
        out_specs=tuple([HBM] * (2 * n)),
        input_output_aliases={i: i for i in range(2 * n)},
        compiler_params=pltpu.CompilerParams(has_side_effects=EFFECT),
    )(*srcs, *lands, *sems, *afters)
    handle["srcs"] = list(out[:n])
    return list(out[n:])


def _tie(x, token, name):
    def body(x_ref, t_ref, o_ref):
        pass

    return pl.pallas_call(
        body, name=name, out_shape=jax.ShapeDtypeStruct(x.shape, x.dtype),
        in_specs=[ANY, ANY], out_specs=ANY, input_output_aliases={0: 0},
    )(x, token)


def _forward_to_sibling(land, name, rows):
    def body(land_ref, out_ref, send_sems, recv_sems):
        x, y, c = _place()
        sibling = (x, y, 1 - c)
        chips = [(1 - x, y), (x, 1 - y), (1 - x, 1 - y)]
        piece = pl.ds(rows[0], rows[1] - rows[0])

        def copy(j, core):
            blk = _slot((*chips[j], core))
            return pltpu.make_async_remote_copy(src_ref=land_ref.at[blk, piece], dst_ref=out_ref.at[blk, piece],
                                                send_sem=send_sems.at[j], recv_sem=recv_sems.at[j],
                                                device_id=sibling, device_id_type=MESH)

        sends = [copy(j, c) for j in range(3)]
        for cp in sends:
            cp.start()
        for j in range(3):
            copy(j, 1 - c).wait_recv()
        for cp in sends:
            cp.wait_send()

    return pl.pallas_call(
        body, name=name, in_specs=[ANY], out_specs=ANY, input_output_aliases={0: 0},
        out_shape=jax.ShapeDtypeStruct(land.shape, land.dtype),
        scratch_shapes=[pltpu.SemaphoreType.DMA((3,)), pltpu.SemaphoreType.DMA((3,))],
    )(land)


def _mod_fwd(a, w, b):
    def body(a_ref, w_ref, b_ref, o_ref):
        o_ref[...] = _dot(_silu(a_ref[...]), w_ref[...], NN, precision=HI) + b_ref[...]

    return pl.pallas_call(
        body, name="mod_fwd", out_shape=jax.ShapeDtypeStruct((a.shape[0], w.shape[1]), F32),
        compiler_params=pltpu.CompilerParams(vmem_limit_bytes=VMEM_LIMIT),
    )(a, w, b)


def _mod_bwd(a, d, w):
    def body(a_ref, d_ref, w_ref, dw_ref, dc_ref):
        av = a_ref[...]
        dv = d_ref[...]
        dw_ref[...] = _dot(_silu(av), dv, TN, precision=HI)
        da = _dot(dv[0:8, :], w_ref[...], NT, precision=HI) * _dsilu(av[0:8, :])
        row = lax.broadcasted_iota(jnp.int32, da.shape, 0)
        dc_ref[...] = jnp.where(row == 0, da, 0.0)

    return pl.pallas_call(
        body, name="mod_bwd",
        out_shape=[jax.ShapeDtypeStruct(w.shape, F32), jax.ShapeDtypeStruct((8, w.shape[0]), F32)],
        compiler_params=pltpu.CompilerParams(vmem_limit_bytes=VMEM_LIMIT),
    )(a, d, w)


def _sum_devices(g):
    def body(g_ref, o_ref):
        acc = g_ref[0]
        for i in range(1, g.shape[0]):
            acc = acc + g_ref[i]
        o_ref[...] = acc

    return pl.pallas_call(body, name="sum_devices_%d" % g.shape[1],
                          out_shape=jax.ShapeDtypeStruct(g.shape[1:], F32))(g)


def _sum_windows(g, name):
    n, r, c = g.shape
    tr = 128

    def body(g_ref, o_ref):
        x, y, cc = _place()
        lane0 = (W_IN_SHARD * _slot((x, y, cc))) % DH
        acc = g_ref[0].astype(F32)
        for i in range(1, n):
            acc = acc + g_ref[i].astype(F32)
        o_ref[...] = pltpu.roll(acc, (c - lane0) % c, 1).T

    return pl.pallas_call(
        body, name=name, grid=(r // tr,),
        in_specs=[pl.BlockSpec((n, tr, c), lambda i: (0, i, 0))],
        out_specs=pl.BlockSpec((c, tr), lambda i: (0, i)),
        out_shape=jax.ShapeDtypeStruct((c, r), F32),
        compiler_params=_cp("parallel"),
    )(g)


def _adam_rows(r, c, n):
    budget = 10 * 1024 * 1024
    best = None
    for tr in range(16, r + 1, 16):
        if r % tr == 0 and tr * c * (2 * n + 28) <= budget:
            best = tr
    return best if best is not None else r


def _adamw(g, w, m, v, name):
    n, r, c = g.shape
    tr = _adam_rows(r, c, n)
    bc1 = 1.0 - ADAM_B1 ** ADAM_STEP
    bc2 = 1.0 - ADAM_B2 ** ADAM_STEP

    def body(g_ref, w_ref, m_ref, v_ref, go_ref, d_ref, mo_ref, vo_ref):
        grad = g_ref[0].astype(F32)
        for i in range(1, n):
            grad = grad + g_ref[i].astype(F32)
        go_ref[...] = grad
        m_new = ADAM_B1 * m_ref[...] + (1.0 - ADAM_B1) * grad
        v_new = ADAM_B2 * v_ref[...] + (1.0 - ADAM_B2) * (grad * grad)
        mo_ref[...] = m_new
        vo_ref[...] = v_new
        d_ref[...] = -ADAM_LR * ((m_new / bc1) / (jnp.sqrt(v_new / bc2) + ADAM_EPS) + ADAM_WD * w_ref[...])

    blk = pl.BlockSpec((tr, c), lambda i: (i, 0))
    out = jax.ShapeDtypeStruct((r, c), F32)
    return pl.pallas_call(
        body, name=name, grid=(r // tr,),
        in_specs=[pl.BlockSpec((n, tr, c), lambda i: (0, i, 0)), blk, blk, blk],
        out_specs=[blk] * 4, out_shape=[out] * 4,
        compiler_params=_cp("parallel"),
    )(g, w, m, v)


ADAM_ROWS3 = 168


def _adam_math(grad, w, m, v):
    bc1 = 1.0 - ADAM_B1 ** ADAM_STEP
    bc2 = 1.0 - ADAM_B2 ** ADAM_STEP
    m_new = ADAM_B1 * m + (1.0 - ADAM_B1) * grad
    v_new = ADAM_B2 * v + (1.0 - ADAM_B2) * (grad * grad)
    delta = -ADAM_LR * ((m_new / bc1) / (jnp.sqrt(v_new / bc2) + ADAM_EPS) + ADAM_WD * w)
    return delta, m_new, v_new


def _adamw_rows3(g, w3, m3, v3, name, cols, prev):
    r, _, _ = w3.shape
    c = cols[1] - cols[0]
    n = min(-(-r // 16) * 8, ADAM_ROWS3 * D // c // 8 * 8)
    starts = list(range(0, r - n, n)) + [r - n]
    held = [] if prev is None else list(prev)

    def body(g_hbm, w_hbm, m_hbm, v_hbm, *refs):
        go_hbm, d_hbm, mo_hbm, vo_hbm, gbuf, ibuf, obuf, in_sems, out_sems = refs[len(held):]
        part = lambda h, r0: h.at[pl.ds(r0, n), 0, pl.ds(cols[0], c)]

        def fetch(p):
            r0, slot = starts[p], p % 2
            g0 = (r0 // 8) * 8
            cps = [pltpu.make_async_copy(g_hbm.at[pl.ds(g0, n + 8)], gbuf.at[slot], in_sems.at[slot, 0])]
            cps += [pltpu.make_async_copy(part(h, r0), ibuf.at[slot, k], in_sems.at[slot, 1 + k])
                    for k, h in enumerate((w_hbm, m_hbm, v_hbm))]
            for cp in cps:
                cp.start()
            return cps

        pending, outs = fetch(0), []
        for p, r0 in enumerate(starts):
            slot = p % 2
            nxt = fetch(p + 1) if p + 1 < len(starts) else []
            for cp in pending:
                cp.wait()
            grad = gbuf[slot, pl.ds(r0 - (r0 // 8) * 8, n), :]
            delta, m_new, v_new = _adam_math(grad, ibuf[slot, 0], ibuf[slot, 1], ibuf[slot, 2])
            for cp in outs:
                cp.wait()
            for k, val in enumerate((grad, delta, m_new, v_new)):
                obuf[slot, k] = val
            outs = [pltpu.make_async_copy(obuf.at[slot, k], part(h, r0), out_sems.at[slot, k])
                    for k, h in enumerate((go_hbm, d_hbm, mo_hbm, vo_hbm))]
            for cp in outs:
                cp.start()
            pending = nxt
        for cp in outs:
            cp.wait()

    out = jax.ShapeDtypeStruct(w3.shape, F32)
    return pl.pallas_call(
        body, name=name, in_specs=[ANY] * (4 + len(held)), out_specs=[ANY] * 4, out_shape=[out] * 4,
        input_output_aliases={4 + k: k for k in range(len(held))},
        scratch_shapes=[pltpu.VMEM((2, n + 8, c), F32), pltpu.VMEM((2, 3, n, c), F32), pltpu.VMEM((2, 4, n, c), F32),
                        pltpu.SemaphoreType.DMA((2, 4)), pltpu.SemaphoreType.DMA((2, 4))],
        compiler_params=pltpu.CompilerParams(vmem_limit_bytes=VMEM_LIMIT),
    )(g, w3, m3, v3, *held)


def kernel(x, c, ctx, c_ctx, w_mod, b_mod, norm_pre1, norm_post1, norm_pre2, norm_post2, w_in, hg_lb, hg_onorm, gla_w_gk, gla_b_gk, gla_onorm, w_br_hg, w_br_gla, w_out, w_ff_gate, w_ff_up, w_ff_down, loss_target, m_c_ctx, m_w_mod, m_b_mod, m_norm_pre1, m_norm_post1, m_norm_pre2, m_norm_post2, m_w_in, m_hg_lb, m_hg_onorm, m_gla_w_gk, m_gla_b_gk, m_gla_onorm, m_w_br_hg, m_w_br_gla, m_w_out, m_w_ff_gate, m_w_ff_up, m_w_ff_down, v_c_ctx, v_w_mod, v_b_mod, v_norm_pre1, v_norm_post1, v_norm_pre2, v_norm_post2, v_w_in, v_hg_lb, v_hg_onorm, v_gla_w_gk, v_gla_b_gk, v_gla_onorm, v_w_br_hg, v_w_br_gla, v_w_out, v_w_ff_gate, v_w_ff_up, v_w_ff_down):
    xi, yi, ci = lax.axis_index("x"), lax.axis_index("y"), lax.axis_index("c")
    me = 4 * xi + 2 * yi + ci
    t = CTX + x.shape[1]

    w_in_pieces, w_in_state = [], {}

    def w_in_piece(i):
        return (_view_near_rows((i * W_IN_PIECE, (i + 1) * W_IN_PIECE)), w_in_state["src"], w_in_state["land"])

    def started_w_in(handle):
        w_in_state.update(src=handle["srcs"], land=handle["lands"])
        w_in_pieces.append(handle)

    tr_ = lambda a: jnp.swapaxes(a[0], 0, 1)
    w_in_bf = jnp.pad(w_in[0].astype(BF16), ((0, 0), (0, W_IN_PAD - W_IN_SHARD)))
    w_in_state.update(src=[w_in_bf], land=[lax.empty((N_DEV,) + w_in_bf.shape, BF16)])
    gathered = lambda arrs: [(N_DEV,) + a.shape for a in arrs]
    whole = lambda arrs: (_view_whole, arrs, gathered(arrs))
    small_in = [c, hg_lb, gla_w_gk[0], gla_b_gk[0]]
    (small_handle, piece), tok = _split_start([whole(small_in), w_in_piece(0)], "ag_small_start", c)
    started_w_in(piece)
    c_all, lb_g, wgk_g, bgk_g = _split_wait(small_handle, "ag_small_wait", tok)
    big = [w_in[0], w_br_hg[0], w_br_gla[0], w_out[0], tr_(w_ff_gate), tr_(w_ff_up), w_ff_down[0]]
    big_bf = [None] + [w.astype(BF16) for w in big[1:]]
    cols = lambda g: jnp.transpose(g, (1, 0, 2)).reshape(g.shape[1], N_DEV * g.shape[2])

    def get_w_in(after):
        w_full, first = None, 0
        for s, last in enumerate(W_IN_STAGES):
            for i in range(first, last):
                land = _split_wait(w_in_pieces[i], "ag_w_in_wait%d" % i, after if w_full is None else [after, w_full],
                                   srcs=w_in_state["src"], lands=w_in_state["land"])
                w_in_state.update(src=w_in_pieces[i]["srcs"], land=land)
            rows = (first * W_IN_PIECE, last * W_IN_PIECE)
            w_in_state["land"] = [_forward_to_sibling(w_in_state["land"][0], "ag_w_in_forward%d" % s, rows)]
            w_full = _assemble_w_in(w_in_state["land"][0], rows, w_full, "assemble_w_in%d" % s)
            first = last
        return w_full

    def get_mix(after):
        g_brh, g_brg, g_out = _split_wait(mix_handle, "ag_mix_wait", after)
        return _gate_cols(cols(g_brh)), _gate_cols(cols(g_brg)), _gate_rows(g_out.reshape(D, D))

    def get_ffn(after):
        g_gate, g_up, g_down = _split_wait(ffn_handle, "ag_ffn_wait", after)
        return (g_gate.reshape(D_FF, D), g_up.reshape(D_FF, D)), g_down.reshape(D_FF, D)

    hg_lb_full = jnp.transpose(lb_g, (1, 2, 0, 3)).reshape(2, 2, HW)
    wgk_k = _layout_wgk(jnp.transpose(wgk_g, (1, 2, 0, 3)).reshape(2, 16, HW)).astype(BF16)
    bgk_k = jnp.transpose(bgk_g, (1, 0, 2)).reshape(1, D)
    onw = jnp.concatenate([jnp.tile(hg_onorm, (1, NH // 2)), jnp.tile(gla_onorm, (1, NH // 2))], axis=1)

    n_mod = w_mod.shape[2]
    a9 = jnp.concatenate([c_ctx[None], c_all[:, 0], jnp.zeros((16 - 1 - N_DEV, D), F32)], axis=0)
    b_loc = lax.dynamic_slice(b_mod, (0, me * n_mod), (1, n_mod))
    s_loc = _mod_fwd(a9, w_mod[0], b_loc)
    (mod_handle, piece), tok = _split_start([whole([s_loc]), w_in_piece(1)], "ag_mod_start", s_loc)
    started_w_in(piece)
    for i in range(2, D // W_IN_PIECE):
        (piece,), tok = _split_start([w_in_piece(i)], "ag_w_in_start%d" % i, tok)
        started_w_in(piece)
    s_all, = _split_wait(mod_handle, "ag_mod_wait", tok)
    mod_all = jnp.transpose(s_all, (1, 0, 2)).reshape(16, N_DEV * n_mod)
    pad8 = lambda m: jnp.concatenate([m.reshape(6, D), jnp.zeros((2, D), F32)], axis=0)
    modc = pad8(mod_all[0])
    modx = pad8(lax.dynamic_slice(mod_all, (1 + me, 0), (1, N_DEV * n_mod))[0])

    (mix_handle, ffn_handle), tok = _split_start([whole(big_bf[1:4]), whole(big_bf[4:])], "ag_big_start", s_all)

    z = (ctx[0], x[0])
    modx = _tie(modx, tok, "tie_mod")
    norms = (norm_pre1, norm_post1, norm_pre2, norm_post2)
    shard = lambda d: jnp.transpose(d.reshape(d.shape[0], N_DEV, -1), (1, 0, 2)).astype(BF16)
    rowshard = lambda d: d.reshape(N_DEV, d.shape[0] // N_DEV, d.shape[1]).astype(BF16)
    sent, w_in_grad = [], {}

    def w_in_chunk(i):
        half, rows = W_IN_GRAD_CHUNKS[i]
        return (_view_window(rows), w_in_grad[half], [(N_DEV, rows[1] - rows[0], D)])

    def sent_w_in(i, handle):
        w_in_grad[W_IN_GRAD_CHUNKS[i][0]] = handle["srcs"]
        sent.append(("w_in%d" % i, ["w_in#%d" % i], handle))

    def send(names, grads, x_after):
        if names == ("w_in_a",):
            w_in_grad["a"] = list(grads)
            (handle,), tok = _split_start([w_in_chunk(0)], "grads_w_in0_start", x_after)
            sent_w_in(0, handle)
            return _tie(x_after, tok, "tie_w_in0")
        if names == ("w_in_b",):
            w_in_grad["b"] = list(grads)
            return x_after
        arrs, leaves = [], []
        for nm, g in zip(names, grads):
            if nm in ("w_gate_t", "w_up_t"):
                arrs.append(rowshard(g))
                leaves.append({"w_gate_t": "w_ff_gate", "w_up_t": "w_ff_up"}[nm])
            elif nm == "w_down":
                arrs.append(rowshard(g))
                leaves.append("w_ff_down")
            elif nm == "w_out":
                arrs.append(rowshard(g[GOFF:GOFF + D]))
                leaves.append(nm)
            else:
                arrs.append(shard(g[:, GOFF:GOFF + D]))
                leaves.append(nm)
        (handle,), tok = _split_start([(_view_block, arrs, [a.shape for a in arrs])], "grads_%s_start" % names[0],
                                      x_after)
        sent.append((names[0], leaves, handle))
        return _tie(x_after, tok, "tie_" + names[0])

    r = _local_step(z, loss_target[0], modc, modx, norms, onw, hg_lb_full, wgk_k, bgk_k,
                    get_w_in, get_mix, get_ffn, send)
    grad_x = r["grad_x"][None]

    sm_pre, sm_mid, sm_fin = r["sm_pre"], r["sm_mid"], r["sm_final"]
    dmodc = jnp.stack([sm_pre[0], sm_pre[2], sm_mid[4], sm_mid[0], sm_mid[2], sm_fin[0]]).reshape(-1)
    dmodx = jnp.stack([sm_pre[1], sm_pre[3], sm_mid[5], sm_mid[1], sm_mid[3], sm_fin[1]]).reshape(-1)
    on = r["sm_post"][0].reshape(NH, DH)
    pieces = [dmodc, dmodx, sm_pre[4], sm_mid[7], sm_mid[6], sm_fin[2], on[:NH // 2].sum(0), on[NH // 2:].sum(0),
              r["d_lb"][:2].reshape(-1), _unlayout_wgk(r["d_wgk"]).reshape(-1), r["d_bgk"][0]]
    loss_local = (0.5 / D) * jnp.sum(r["loss_vec"])
    pieces.append(jnp.concatenate([loss_local.reshape(1), jnp.zeros((DH - 1,), F32)]))
    sizes = [p.shape[0] for p in pieces]
    pack = jnp.concatenate(pieces).reshape(-1, DH)
    moms = [(m_w_in, v_w_in), (m_w_br_hg, v_w_br_hg), (m_w_br_gla, v_w_br_gla), (m_w_out, v_w_out),
            (m_w_ff_gate, v_w_ff_gate), (m_w_ff_up, v_w_ff_up), (m_w_ff_down, v_w_ff_down)]
    names = ["w_in", "w_br_hg", "w_br_gla", "w_out", "w_ff_gate", "w_ff_up", "w_ff_down"]
    wmv = {nm: (w, m, v) for nm, w, (m, v) in zip(names, big, moms)}
    res = {}

    def update(nm):
        w, m, v = wmv[nm]
        if nm in ("w_ff_gate", "w_ff_up"):
            outs = _adamw(recv[nm], w, tr_(m), tr_(v), "adamw_" + nm)
            res[nm] = [jnp.swapaxes(o, 0, 1)[None] for o in outs]
        else:
            res[nm] = [o[None] for o in _adamw(recv[nm], w, m[0], v[0], "adamw_" + nm)]

    (small_handle, handle), tok = _split_start([whole([pack]), w_in_chunk(1)], "small_grads_start", pack)
    sent_w_in(1, handle)
    recv = {}
    for first, leaves, handle in sent:
        if not first.startswith("w_in"):
            recv.update(zip(leaves, _split_wait(handle, "grads_%s_wait" % first, tok)))
    update("w_ff_gate")
    update("w_ff_up")
    pack_all, = _split_wait(small_handle, "small_grads_wait", [res["w_ff_gate"][0], res["w_ff_up"][0]])
    tot = _sum_devices(pack_all).reshape(-1)
    offs = [sum(sizes[:i]) for i in range(len(sizes))]
    part = lambda i: tot[offs[i]:offs[i] + sizes[i]]
    dmodc_t, dmodx_t = part(0), part(1)
    g_b_mod = (dmodc_t + dmodx_t)[None]
    g_norms = [part(i)[None] for i in (2, 3, 4, 5)]
    g_hg_on, g_gla_on = part(6)[None], part(7)[None]
    lb0 = lax.dynamic_slice(part(8).reshape(2, HW), (0, me * (HW // N_DEV)), (2, HW // N_DEV))
    g_hg_lb = jnp.stack([lb0, -lb0])
    g_wgk = lax.dynamic_slice(part(9).reshape(2, 16, HW), (0, 0, me * (HW // N_DEV)), (2, 16, HW // N_DEV))[None]
    g_bgk = lax.dynamic_slice(part(10).reshape(2, HW), (0, me * (HW // N_DEV)), (2, HW // N_DEV))[None]
    loss = part(11)[0]

    dmx_all = pack_all.reshape(N_DEV, -1)[:, sizes[0]:sizes[0] + sizes[1]]
    d9 = jnp.concatenate([lax.dynamic_slice(dmodc_t[None], (0, me * n_mod), (1, n_mod)),
                          lax.dynamic_slice(dmx_all, (0, me * n_mod), (N_DEV, n_mod)),
                          jnp.zeros((16 - 1 - N_DEV, n_mod), F32)], axis=0)
    g_w_mod, dcc_part = _mod_bwd(a9, d9, w_mod[0])
    (cctx_handle, handle), tok = _split_start([whole([dcc_part]), w_in_chunk(2)], "c_ctx_start", dcc_part)
    sent_w_in(2, handle)
    recv["w_ff_down"] = _tie(recv["w_ff_down"], tok, "tie_down")
    update("w_ff_down")
    res["w_mod"] = [o[None] for o in _adamw(g_w_mod[None], w_mod[0], m_w_mod[0], v_w_mod[0], "adamw_w_mod")]
    for nm in ("w_out", "w_br_hg", "w_br_gla"):
        update(nm)
    dcc_all, = _split_wait(cctx_handle, "c_ctx_wait", [res["w_ff_down"][0], res["w_mod"][0]])
    g_c_ctx = _sum_devices(dcc_all)[0]

    small = [("c_ctx", c_ctx, m_c_ctx, v_c_ctx, g_c_ctx), ("b_mod", b_mod, m_b_mod, v_b_mod, g_b_mod),
             ("norm_pre1", norm_pre1, m_norm_pre1, v_norm_pre1, g_norms[0]),
             ("norm_post1", norm_post1, m_norm_post1, v_norm_post1, g_norms[1]),
             ("norm_pre2", norm_pre2, m_norm_pre2, v_norm_pre2, g_norms[2]),
             ("norm_post2", norm_post2, m_norm_post2, v_norm_post2, g_norms[3]),
             ("hg_lb", hg_lb, m_hg_lb, v_hg_lb, g_hg_lb), ("hg_onorm", hg_onorm, m_hg_onorm, v_hg_onorm, g_hg_on),
             ("gla_w_gk", gla_w_gk, m_gla_w_gk, v_gla_w_gk, g_wgk), ("gla_b_gk", gla_b_gk, m_gla_b_gk, v_gla_b_gk, g_bgk),
             ("gla_onorm", gla_onorm, m_gla_onorm, v_gla_onorm, g_gla_on)]
    flat = lambda k: jnp.concatenate([s[k].reshape(-1) for s in small]).reshape(-1, DH)
    outs = _adamw(flat(4)[None], flat(1), flat(2), flat(3), "adamw_small")
    off = 0
    for nm, w, _, _, _ in small:
        res[nm] = [o.reshape(-1)[off:off + w.size].reshape(w.shape) for o in outs]
        off += w.size

    done = [res[nm][0] for nm in names[1:]] + [res["w_mod"][0]] + [o for nm, *_ in small for o in res[nm]]
    major = lambda a: jnp.transpose(a, (2, 0, 1))
    outs, row0 = None, 0
    for i, (first, leaves, handle) in enumerate(s for s in sent if s[0].startswith("w_in")):
        half = W_IN_GRAD_CHUNKS[i][0]
        land, = _split_wait(handle, "grads_%s_wait" % first, done, srcs=w_in_grad[half])
        w_in_grad[half] = handle["srcs"]
        rows = (row0, row0 + land.shape[1])
        outs = _adamw_rows3(_sum_windows(land, "sum_windows%d" % i), major(w_in), major(m_w_in), major(v_w_in),
                            "adamw_w_in%d" % i, rows, outs)
        row0 = rows[1]
    res["w_in"] = [jnp.transpose(o, (1, 2, 0)) for o in outs]

    order = ["c_ctx", "w_mod", "b_mod", "norm_pre1", "norm_post1", "norm_pre2", "norm_post2", "w_in", "hg_lb",
             "hg_onorm", "gla_w_gk", "gla_b_gk", "gla_onorm", "w_br_hg", "w_br_gla", "w_out", "w_ff_gate", "w_ff_up",
             "w_ff_down"]
    return (loss, grad_x, *[res[n][k] for k in range(4) for n in order])
```

```python
import functools

import jax
import jax.numpy as jnp
from jax import lax
from jax.experimental import pallas as pl
from jax.experimental.pallas import tpu as pltpu

F32 = jnp.float32
BF16 = jnp.bfloat16
HI = lax.Precision.HIGHEST

N_DEV = 8
D = 1024
CTX = 256
HW = 512
DH = 128
NH = 8
D_FF = 2816
EPS = 1e-6
GLA_NORM = 16.0
CHUNK = 64
TR = 256
NCT = CTX // TR
W_IN_COLS = 7168
MAIN0 = 0
LR0 = 4608
GW = 1152
GOFF = 32
GATE_HG0 = LR0
GATE_GLA0 = LR0 + D
LEVELS = (32, 16, 8)
EXP_CLAMP = 80.0
VMEM_LIMIT = 48 * 1024 * 1024

ADAM_LR, ADAM_B1, ADAM_B2, ADAM_EPS, ADAM_WD, ADAM_STEP = 0.001, 0.9, 0.999, 1e-08, 0.01, 10


def _cp(*sem):
    return pltpu.CompilerParams(dimension_semantics=sem, vmem_limit_bytes=VMEM_LIMIT)


def _sig(x):
    return jax.nn.sigmoid(x)


def _silu(x):
    return x * _sig(x)


def _dsilu(x):
    s = _sig(x)
    return s * (1.0 + x * (1.0 - s))


def _rstd(x):
    return lax.rsqrt(jnp.mean(x * x, axis=-1, keepdims=True) + EPS)


def _rms_bwd(a, y, r):
    return r * (a - y * (r * r) * jnp.mean(a * y, axis=-1, keepdims=True))


def _colsum(x):
    return jnp.sum(x, axis=0, keepdims=True)


def _dot(a, b, dims, precision=None):
    return lax.dot_general(a, b, (dims, ((), ())), preferred_element_type=F32, precision=precision)


NN = ((1,), (0,))
NT = ((1,), (1,))
TN = ((0,), (0,))

SCAN_HEADS_FWD = 4
SCAN_HEADS_BWD = 4


def _split_dot(m, x):
    mb = m.astype(BF16)
    x1 = x.astype(BF16)
    r1 = x - x1.astype(F32)
    x2 = r1.astype(BF16)
    x3 = (r1 - x2.astype(F32)).astype(BF16)
    return _dot(mb, x1, NN) + _dot(mb, x2, NN) + _dot(mb, x3, NN)


def _matmul(a, b, dims, out_dtype, name, tm, tn, tk, a_off=0, m_out=None):
    a_pair = isinstance(a, (tuple, list))
    as_ = list(a) if a_pair else [a]
    a = as_[0]
    pair = isinstance(b, (tuple, list))
    bs = list(b) if pair else [b]
    b1 = bs[0]
    rows = b1.shape[0] * len(bs)
    half = None
    if dims == NN:
        m, k, n = a.shape[0], rows, b1.shape[1]
        a_spec = pl.BlockSpec((tm, tk), lambda i, j, kk: (i, kk + a_off))
        half = b1.shape[0] // tk
        if a_pair:
            assert pair and a.shape[1] == b1.shape[0] and a_off == 0
            a_spec = [pl.BlockSpec((tm, tk), lambda i, j, kk: (i, jnp.minimum(kk, half - 1))),
                      pl.BlockSpec((tm, tk), lambda i, j, kk: (i, jnp.maximum(kk - half, 0)))]
        b_maps = [lambda i, j, kk: (kk, j)] if not pair else [
            lambda i, j, kk: (jnp.minimum(kk, half - 1), j), lambda i, j, kk: (jnp.maximum(kk - half, 0), j)]
        b_specs = [pl.BlockSpec((tk, tn), f) for f in b_maps]
        axis = 2
    elif dims == NT:
        m, k, n = a.shape[0], b1.shape[1], rows
        a_spec = pl.BlockSpec((tm, tk), lambda i, j, kk: (i, kk + a_off))
        half = b1.shape[0] // tn
        b_maps = [lambda i, j, kk: (j, kk)] if not pair else [
            lambda i, j, kk: (jnp.minimum(j, half - 1), kk), lambda i, j, kk: (jnp.maximum(j - half, 0), kk)]
        b_specs = [pl.BlockSpec((tn, tk), f) for f in b_maps]
        axis = 1
    else:
        assert not pair
        m, k = (a.shape[1] if m_out is None else m_out), a.shape[0]
        n = b1.shape[1]
        a_spec = pl.BlockSpec((tk, tm), lambda i, j, kk: (kk, i + a_off))
        b_specs = [pl.BlockSpec((tk, tn), lambda i, j, kk: (kk, j))]
    assert m % tm == 0 and n % tn == 0 and k % tk == 0, (name, m, n, k, tm, tn, tk)
    nk = k // tk
    nb = len(bs)
    na = len(as_)
    assert na == 1 or dims == NN

    def body(*refs):
        a_refs, refs = refs[:na], refs[na:]
        o_ref = refs[nb]
        if pair:
            bv = jnp.where(pl.program_id(axis) < half, refs[0][...], refs[1][...])
        else:
            bv = refs[0][...]
        av = a_refs[0][...] if na == 1 else jnp.where(pl.program_id(2) < half, a_refs[0][...], a_refs[1][...])
        part = _dot(av, bv, dims)
        if nk == 1:
            o_ref[...] = part.astype(o_ref.dtype)
            return
        acc_ref = refs[nb + 1]
        kk = pl.program_id(2)

        @pl.when(kk == 0)
        def _():
            acc_ref[...] = part

        @pl.when(kk > 0)
        def _():
            acc_ref[...] += part

        @pl.when(kk == nk - 1)
        def _():
            o_ref[...] = acc_ref[...].astype(o_ref.dtype)

    return pl.pallas_call(
        body,
        name=name,
        grid=(m // tm, n // tn, nk),
        in_specs=(a_spec if a_pair else [a_spec]) + b_specs,
        out_specs=pl.BlockSpec((tm, tn), lambda i, j, kk: (i, j)),
        out_shape=jax.ShapeDtypeStruct((m, n), out_dtype),
        scratch_shapes=[] if nk == 1 else [pltpu.VMEM((tm, tn), F32)],
        compiler_params=_cp("parallel", "parallel", "arbitrary"),
    )(*as_, *bs)


def _mm_gu_act(h, w_gate_t, w_up_t, name, tm):
    t = h.shape[0]
    tn = D_FF // 2

    def body(a_ref, bg_ref, bu_ref, u_ref, v_ref, act_ref):
        a = a_ref[...]
        u = _dot(a, bg_ref[...], NT)
        v = _dot(a, bu_ref[...], NT)
        u_ref[...] = u.astype(BF16)
        v_ref[...] = v.astype(BF16)
        act_ref[...] = (_silu(u) * v).astype(BF16)

    wspec = pl.BlockSpec((tn, D), lambda i, j: (j, 0))
    ospec = pl.BlockSpec((tm, tn), lambda i, j: (i, j))
    out = jax.ShapeDtypeStruct((t, D_FF), BF16)
    return pl.pallas_call(
        body, name=name, grid=(t // tm, D_FF // tn),
        in_specs=[pl.BlockSpec((tm, D), lambda i, j: (i, 0)), wspec, wspec],
        out_specs=[ospec] * 3, out_shape=[out] * 3,
        compiler_params=_cp("parallel", "parallel"),
    )(h, w_gate_t, w_up_t)


def _mm_down_dx_act(dy, w_down, u, v, name, tm):
    t = dy.shape[0]
    tn = D_FF // 2

    def body(a_ref, b_ref, u_ref, v_ref, du_ref, dv_ref):
        dact = _dot(a_ref[...], b_ref[...], NT)
        u = u_ref[...].astype(F32)
        du_ref[...] = (dact * v_ref[...].astype(F32) * _dsilu(u)).astype(BF16)
        dv_ref[...] = (dact * _silu(u)).astype(BF16)

    ospec = pl.BlockSpec((tm, tn), lambda i, j: (i, j))
    out = jax.ShapeDtypeStruct((t, D_FF), BF16)
    return pl.pallas_call(
        body, name=name, grid=(t // tm, D_FF // tn),
        in_specs=[pl.BlockSpec((tm, D), lambda i, j: (i, 0)), pl.BlockSpec((tn, D), lambda i, j: (j, 0)), ospec, ospec],
        out_specs=[ospec] * 2, out_shape=[out] * 2,
        compiler_params=_cp("parallel", "parallel"),
    )(dy, w_down, u, v)


def _row(c):
    return pl.BlockSpec((TR, c), lambda i: (i, 0))


def _rowcol(width, cb):
    return pl.BlockSpec((TR, width), lambda i: (i, cb))


def _full(shape):
    return pl.BlockSpec(shape, lambda i: (0,) * len(shape))


def _mod_row(mc_ref, mx_ref, k, is_ctx):
    return jnp.where(is_ctx, mc_ref[k:k + 1, :], mx_ref[k:k + 1, :])


def _z_specs():
    return [pl.BlockSpec((TR, D), lambda i: (jnp.minimum(i, NCT - 1), 0)),
            pl.BlockSpec((TR, D), lambda i: (jnp.maximum(i - NCT, 0), 0))]


def _z_tile(c_ref, x_ref, is_ctx):
    return jnp.where(is_ctx, c_ref[...], x_ref[...])


def _acc_row(ref, k, val):
    ref[k:k + 1, :] += val


def _acc_mod(ref, k, is_ctx, val):
    zero = jnp.zeros_like(val)
    ref[k:k + 1, :] += jnp.where(is_ctx, val, zero)
    ref[k + 1:k + 2, :] += jnp.where(is_ctx, zero, val)


def _prenorm(z, nw, modc, modx, i_shift, i_scale, name):
    t = z[0].shape[0] + z[1].shape[0]

    def body(zc_ref, zx_ref, nw_ref, mc_ref, mx_ref, h_ref):
        is_ctx = pl.program_id(0) < NCT
        x = _z_tile(zc_ref, zx_ref, is_ctx)
        n = x * _rstd(x) * nw_ref[...]
        h = n * (1.0 + _mod_row(mc_ref, mx_ref, i_scale, is_ctx)) + _mod_row(mc_ref, mx_ref, i_shift, is_ctx)
        h_ref[...] = h.astype(BF16)

    return pl.pallas_call(
        body, name=name, grid=(t // TR,),
        in_specs=_z_specs() + [_full((1, D)), _full((8, D)), _full((8, D))],
        out_specs=_row(D),
        out_shape=jax.ShapeDtypeStruct((t, D), BF16),
        compiler_params=_cp("parallel"),
    )(*z, nw, modc, modx)


def _hg_lb(lb_ref, d):
    a0 = lb_ref[0, d:d + 1, :]
    a1 = lb_ref[1, d:d + 1, :]
    mx = jnp.maximum(a0, a1)
    e0 = jnp.exp(a0 - mx)
    e1 = jnp.exp(a1 - mx)
    return e0 / (e0 + e1)


def _log_sigmoid(x):
    return jnp.minimum(x, 0.0) - jnp.log(1.0 + jnp.exp(-jnp.abs(x)))


def _gates_fwd(p, hg_lb, wgk, bgk):
    t = p.shape[0]
    seg = lambda j: _rowcol(HW, MAIN0 // HW + j)

    def body(hq_ref, hi_ref, hf_ref, hb_ref, gq_ref, gk_ref, gv_ref, lr_ref, lb_ref, wgk_ref, bgk_ref,
             q_ref, v_ref, kf_ref, kb_ref, gf_ref, gb_ref):
        q_ref[:, :HW] = _silu(hq_ref[...].astype(F32)).astype(BF16)
        q_ref[:, HW:] = (gq_ref[...].astype(F32) * (DH ** -0.5)).astype(BF16)
        v_ref[:, :HW] = hi_ref[...]
        v_ref[:, HW:] = gv_ref[...]
        xg = _dot(lr_ref[...].astype(BF16), wgk_ref[...], NN) + bgk_ref[...]
        for d, (raw_ref, k_ref, g_ref) in enumerate(((hf_ref, kf_ref, gf_ref), (hb_ref, kb_ref, gb_ref))):
            lbd = _hg_lb(lb_ref, d)
            f = lbd + (1.0 - lbd) * _sig(raw_ref[...].astype(F32))
            k_ref[:, :HW] = (1.0 - f).astype(BF16)
            k_ref[:, HW:] = gk_ref[...]
            g_ref[:, :HW] = jnp.log(f)
            g_ref[:, HW:] = _log_sigmoid(xg[:, d * HW:(d + 1) * HW]) * (1.0 / GLA_NORM)

    out = jax.ShapeDtypeStruct((t, D), F32)
    outb = jax.ShapeDtypeStruct((t, D), BF16)
    return pl.pallas_call(
        body, name="gates_fwd", grid=(t // TR,),
        in_specs=[seg(0), seg(1), seg(2), seg(3), seg(5), seg(6), seg(7), _rowcol(DH, LR0 // DH),
                  _full((2, 2, HW)), _full((DH, D)), _full((1, D))],
        out_specs=[_row(D)] * 6,
        out_shape=[outb] * 4 + [out] * 2,
        compiler_params=_cp("parallel"),
    )(p, p, p, p, p, p, p, p, hg_lb, wgk, bgk)


def _post_fwd(o_fw, o_bw, p, onw):
    t = o_fw.shape[0]

    def body(of_ref, ob_ref, g1_ref, g2_ref, w_ref, y_ref):
        for h in range(NH):
            sl = slice(h * DH, (h + 1) * DH)
            o = of_ref[:, sl] + ob_ref[:, sl]
            g_ref = g1_ref if h < NH // 2 else g2_ref
            gs = slice((h % (NH // 2)) * DH, (h % (NH // 2) + 1) * DH)
            n = o * _rstd(o) * w_ref[:, sl]
            y_ref[:, sl] = (n * _silu(g_ref[:, gs].astype(F32))).astype(BF16)

    return pl.pallas_call(
        body, name="post_fwd", grid=(t // TR,),
        in_specs=[_row(D), _row(D), _rowcol(HW, MAIN0 // HW + 4), _rowcol(HW, MAIN0 // HW + 8), _full((1, D))],
        out_specs=_row(D),
        out_shape=jax.ShapeDtypeStruct((t, D), BF16),
        compiler_params=_cp("parallel"),
    )(o_fw, o_bw, p, p, onw)


def _gate_window_specs(col0):
    return [_rowcol(HW, col0 // HW), _rowcol(HW, col0 // HW + 1), _rowcol(DH, (col0 + 2 * HW) // DH)]


def _gate_window(refs):
    return jnp.concatenate([r[...].astype(F32) for r in refs], axis=1)


def _branch_merge(y, w_hg, w_gla, p):
    t = y.shape[0]

    def body(y_ref, wh_ref, wg_ref, a0, a1, a2, b0, b1, b2, u1_ref, u2_ref, m_ref):
        u1 = _dot(y_ref[:, :HW], wh_ref[...], NN)
        u2 = _dot(y_ref[:, HW:], wg_ref[...], NN)
        u1_ref[...] = u1.astype(BF16)
        u2_ref[...] = u2.astype(BF16)
        m_ref[...] = (_sig(_gate_window((a0, a1, a2))) * u1 + _sig(_gate_window((b0, b1, b2))) * u2).astype(BF16)

    out = jax.ShapeDtypeStruct((t, GW), BF16)
    return pl.pallas_call(
        body, name="branch_merge", grid=(t // TR,),
        in_specs=[_row(D), _full((HW, GW)), _full((HW, GW))] + _gate_window_specs(GATE_HG0)
        + _gate_window_specs(GATE_GLA0),
        out_specs=[_row(GW)] * 3, out_shape=[out] * 3,
        compiler_params=_cp("parallel"),
    )(y, w_hg, w_gla, p, p, p, p, p, p)


def _mid_fwd(z, y1, nw_post, nw_pre, modc, modx):
    t = y1.shape[0]

    def body(zc_ref, zx_ref, y_ref, wpo_ref, wpr_ref, mc_ref, mx_ref, z1_ref, h_ref):
        is_ctx = pl.program_id(0) < NCT
        y = y_ref[...].astype(F32)
        z1 = _z_tile(zc_ref, zx_ref, is_ctx) + _mod_row(mc_ref, mx_ref, 2, is_ctx) * (y * _rstd(y) * wpo_ref[...])
        z1_ref[...] = z1
        n = z1 * _rstd(z1) * wpr_ref[...]
        h = n * (1.0 + _mod_row(mc_ref, mx_ref, 4, is_ctx)) + _mod_row(mc_ref, mx_ref, 3, is_ctx)
        h_ref[...] = h.astype(BF16)

    return pl.pallas_call(
        body, name="mid_fwd", grid=(t // TR,),
        in_specs=_z_specs() + [_row(D), _full((1, D)), _full((1, D)), _full((8, D)), _full((8, D))],
        out_specs=[_row(D), _row(D)],
        out_shape=[jax.ShapeDtypeStruct((t, D), F32), jax.ShapeDtypeStruct((t, D), BF16)],
        compiler_params=_cp("parallel"),
    )(*z, y1, nw_post, nw_pre, modc, modx)


def _final(z1, y2, target, nw, modc, modx):
    t = z1.shape[0]

    def body(z1_ref, y_ref, tg_ref, w_ref, mc_ref, mx_ref, dz_ref, dy_ref, loss_ref, sm_ref):
        i = pl.program_id(0)
        is_ctx = i < NCT

        @pl.when(i == 0)
        def _():
            loss_ref[...] = jnp.zeros_like(loss_ref)
            sm_ref[...] = jnp.zeros_like(sm_ref)

        g = _mod_row(mc_ref, mx_ref, 5, is_ctx)
        y = y_ref[...].astype(F32)
        r = _rstd(y)
        w = w_ref[...]
        yr = y * r
        n = yr * w
        e = z1_ref[...] + g * n - tg_ref[...]
        lat = jnp.where(is_ctx, 0.0, 1.0)
        loss_ref[...] += lat * _colsum(e * e)
        dz = e * (lat / D)
        dz_ref[...] = dz
        _acc_mod(sm_ref, 0, is_ctx, _colsum(dz * n))
        dn = dz * g
        _acc_row(sm_ref, 2, _colsum(dn * yr))
        dy_ref[...] = _rms_bwd(dn * w, y, r).astype(BF16)

    return pl.pallas_call(
        body, name="final", grid=(t // TR,),
        in_specs=[_row(D), _row(D), pl.BlockSpec((TR, D), lambda i: (jnp.maximum(i - NCT, 0), 0)),
                  _full((1, D)), _full((8, D)), _full((8, D))],
        out_specs=[_row(D), _row(D), _full((1, D)), _full((8, D))],
        out_shape=[jax.ShapeDtypeStruct((t, D), F32), jax.ShapeDtypeStruct((t, D), BF16),
                   jax.ShapeDtypeStruct((1, D), F32), jax.ShapeDtypeStruct((8, D), F32)],
        compiler_params=_cp("arbitrary"),
    )(z1, y2, target, nw, modc, modx)


def _mid_bwd(dh2, dz, z1, y1, nw_post, nw_pre, modc, modx):
    t = z1.shape[0]

    def body(dh_ref, dz_ref, z1_ref, y_ref, wpo_ref, wpr_ref, mc_ref, mx_ref, dzo_ref, dy_ref, sm_ref):
        i = pl.program_id(0)
        is_ctx = i < NCT

        @pl.when(i == 0)
        def _():
            sm_ref[...] = jnp.zeros_like(sm_ref)

        dh = dh_ref[...].astype(F32)
        z1 = z1_ref[...]
        r = _rstd(z1)
        zr = z1 * r
        wpr = wpr_ref[...]
        n = zr * wpr
        _acc_mod(sm_ref, 0, is_ctx, _colsum(dh))
        _acc_mod(sm_ref, 2, is_ctx, _colsum(dh * n))
        dn = dh * (1.0 + _mod_row(mc_ref, mx_ref, 4, is_ctx))
        _acc_row(sm_ref, 6, _colsum(dn * zr))
        dz1 = dz_ref[...] + _rms_bwd(dn * wpr, z1, r)
        dzo_ref[...] = dz1
        y = y_ref[...].astype(F32)
        r1 = _rstd(y)
        yr = y * r1
        wpo = wpo_ref[...]
        g = _mod_row(mc_ref, mx_ref, 2, is_ctx)
        _acc_mod(sm_ref, 4, is_ctx, _colsum(dz1 * (yr * wpo)))
        dn1 = dz1 * g
        _acc_row(sm_ref, 7, _colsum(dn1 * yr))
        dy_ref[...] = _rms_bwd(dn1 * wpo, y, r1).astype(BF16)

    return pl.pallas_call(
        body, name="mid_bwd", grid=(t // TR,),
        in_specs=[_row(D)] * 4 + [_full((1, D)), _full((1, D)), _full((8, D)), _full((8, D))],
        out_specs=[_row(D), _row(D), _full((8, D))],
        out_shape=[jax.ShapeDtypeStruct((t, D), F32), jax.ShapeDtypeStruct((t, D), BF16),
                   jax.ShapeDtypeStruct((8, D), F32)],
        compiler_params=_cp("arbitrary"),
    )(dh2, dz, z1, y1, nw_post, nw_pre, modc, modx)


def _pre_bwd(dh1, dz, z, nw, modc, modx):
    t = dh1.shape[0]

    def body(dh_ref, dz_ref, zc_ref, zx_ref, w_ref, mc_ref, mx_ref, dzo_ref, sm_ref):
        i = pl.program_id(0)
        is_ctx = i < NCT

        @pl.when(i == 0)
        def _():
            sm_ref[...] = jnp.zeros_like(sm_ref)

        dh = dh_ref[...].astype(F32)
        x = _z_tile(zc_ref, zx_ref, is_ctx)
        r = _rstd(x)
        xr = x * r
        w = w_ref[...]
        _acc_mod(sm_ref, 0, is_ctx, _colsum(dh))
        _acc_mod(sm_ref, 2, is_ctx, _colsum(dh * (xr * w)))
        dn = dh * (1.0 + _mod_row(mc_ref, mx_ref, 1, is_ctx))
        _acc_row(sm_ref, 4, _colsum(dn * xr))
        dzo_ref[...] = dz_ref[...] + _rms_bwd(dn * w, x, r)

    return pl.pallas_call(
        body, name="pre_bwd", grid=(t // TR,),
        in_specs=[_row(D)] * 2 + _z_specs() + [_full((1, D)), _full((8, D)), _full((8, D))],
        out_specs=[pl.BlockSpec((TR, D), lambda i: (jnp.maximum(i - NCT, 0), 0)), _full((8, D))],
        out_shape=[jax.ShapeDtypeStruct((t - CTX, D), F32), jax.ShapeDtypeStruct((8, D), F32)],
        compiler_params=_cp("arbitrary"),
    )(dh1, dz, *z, nw, modc, modx)


def _branch_merge_bwd(dm, p, u1, u2, w_hg, w_gla):
    t = dm.shape[0]

    def body(dm_ref, a0, a1, a2, b0, b1, b2, u1_ref, u2_ref, wh_ref, wg_ref, du1_ref, du2_ref, dg_ref, dyh_ref, dyg_ref):
        dm_ = dm_ref[...].astype(F32)
        s1 = _sig(_gate_window((a0, a1, a2)))
        s2 = _sig(_gate_window((b0, b1, b2)))
        du1 = (dm_ * s1).astype(BF16)
        du2 = (dm_ * s2).astype(BF16)
        du1_ref[...] = du1
        du2_ref[...] = du2
        dg_ref[:, :GW] = (dm_ * u1_ref[...].astype(F32) * s1 * (1.0 - s1)).astype(BF16)
        dg_ref[:, GW:] = (dm_ * u2_ref[...].astype(F32) * s2 * (1.0 - s2)).astype(BF16)
        dyh_ref[...] = _dot(du1, wh_ref[...], NT).astype(BF16)
        dyg_ref[...] = _dot(du2, wg_ref[...], NT).astype(BF16)

    return pl.pallas_call(
        body, name="branch_merge_bwd", grid=(t // TR,),
        in_specs=[_row(GW)] + _gate_window_specs(GATE_HG0) + _gate_window_specs(GATE_GLA0)
        + [_row(GW), _row(GW), _full((HW, GW)), _full((HW, GW))],
        out_specs=[_row(GW), _row(GW), _row(2 * GW), _row(HW), _row(HW)],
        out_shape=[jax.ShapeDtypeStruct((t, GW), BF16), jax.ShapeDtypeStruct((t, GW), BF16),
                   jax.ShapeDtypeStruct((t, 2 * GW), BF16), jax.ShapeDtypeStruct((t, HW), BF16),
                   jax.ShapeDtypeStruct((t, HW), BF16)],
        compiler_params=_cp("parallel"),
    )(dm, p, p, p, p, p, p, u1, u2, w_hg, w_gla)


def _post_bwd(dy_hg, dy_gla, o_fw, o_bw, p, onw):
    t = o_fw.shape[0]

    def body(d1_ref, d2_ref, of_ref, ob_ref, g1_ref, g2_ref, w_ref, do_ref, dg_ref, sm_ref):
        @pl.when(pl.program_id(0) == 0)
        def _():
            sm_ref[...] = jnp.zeros_like(sm_ref)

        for h in range(NH):
            sl = slice(h * DH, (h + 1) * DH)
            gs = slice((h % (NH // 2)) * DH, (h % (NH // 2) + 1) * DH)
            g_ref, d_ref = (g1_ref, d1_ref) if h < NH // 2 else (g2_ref, d2_ref)
            o = of_ref[:, sl] + ob_ref[:, sl]
            r = _rstd(o)
            orr = o * r
            w = w_ref[:, sl]
            gt = g_ref[:, gs].astype(F32)
            dy = d_ref[:, gs].astype(F32)
            dg_ref[:, sl] = (dy * (orr * w) * _dsilu(gt)).astype(BF16)
            dn = dy * _silu(gt)
            sm_ref[0:1, sl] += _colsum(dn * orr)
            do_ref[:, sl] = _rms_bwd(dn * w, o, r)

    return pl.pallas_call(
        body, name="post_bwd", grid=(t // TR,),
        in_specs=[_row(HW), _row(HW), _row(D), _row(D), _rowcol(HW, MAIN0 // HW + 4), _rowcol(HW, MAIN0 // HW + 8),
                  _full((1, D))],
        out_specs=[_row(D), _row(D), _full((8, D))],
        out_shape=[jax.ShapeDtypeStruct((t, D), F32), jax.ShapeDtypeStruct((t, D), BF16),
                   jax.ShapeDtypeStruct((8, D), F32)],
        compiler_params=_cp("arbitrary"),
    )(dy_hg, dy_gla, o_fw, o_bw, p, p, onw)


def _gates_bwd(p, hg_lb, wgk, bgk, dgm, dgo, dq_f, dq_b, dv_f, dv_b, dk_f, dk_b, dg_f, dg_b):
    t = p.shape[0]
    seg = lambda j: _rowcol(HW, MAIN0 // HW + j)

    def body(hq_ref, hf_ref, hb_ref, lr_ref, lb_ref, wgk_ref, bgk_ref, dgm_ref, dgo_ref,
             dqf_ref, dqb_ref, dvf_ref, dvb_ref, dkf_ref, dkb_ref, dgf_ref, dgb_ref,
             dp_ref, dlb_ref, dw_ref, db_ref):
        @pl.when(pl.program_id(0) == 0)
        def _():
            dlb_ref[...] = jnp.zeros_like(dlb_ref)
            dw_ref[...] = jnp.zeros_like(dw_ref)
            db_ref[...] = jnp.zeros_like(db_ref)

        c0 = MAIN0

        def put(j, val):
            dp_ref[:, c0 + j * HW:c0 + (j + 1) * HW] = val.astype(BF16)

        dq = dqf_ref[...].astype(F32) + dqb_ref[...].astype(F32)
        dv = dvf_ref[...].astype(F32) + dvb_ref[...].astype(F32)
        put(0, dq[:, :HW] * _dsilu(hq_ref[...].astype(F32)))
        put(1, dv[:, :HW])
        put(5, dq[:, HW:] * (DH ** -0.5))
        put(7, dv[:, HW:])
        put(6, dkf_ref[:, HW:].astype(F32) + dkb_ref[:, HW:].astype(F32))
        dp_ref[:, c0 + 4 * HW:c0 + 5 * HW] = dgo_ref[:, :HW]
        dp_ref[:, c0 + 8 * HW:c0 + 9 * HW] = dgo_ref[:, HW:]
        lr = lr_ref[...].astype(BF16)
        xg = _dot(lr, wgk_ref[...], NN) + bgk_ref[...]
        dxg = []
        for d, (raw_ref, dk_ref, dg_ref) in enumerate(((hf_ref, dkf_ref, dgf_ref), (hb_ref, dkb_ref, dgb_ref))):
            lbd = _hg_lb(lb_ref, d)
            s = _sig(raw_ref[...].astype(F32))
            f = lbd + (1.0 - lbd) * s
            df = dg_ref[:, :HW] / f - dk_ref[:, :HW].astype(F32)
            put(2 + d, df * (1.0 - lbd) * s * (1.0 - s))
            dlb_ref[d:d + 1, :] += _colsum(df * (1.0 - s)) * (lbd * (1.0 - lbd))
            dxg.append(dg_ref[:, HW:] * (1.0 / GLA_NORM) * _sig(-xg[:, d * HW:(d + 1) * HW]))
        dxg = jnp.concatenate(dxg, axis=1)
        db_ref[0:1, :] += _colsum(dxg)
        dxg_b = dxg.astype(BF16)
        dw_ref[...] += _dot(lr, dxg_b, TN)
        dlr = _dot(dxg_b, wgk_ref[...], NT)
        dp_ref[:, LR0:LR0 + DH] = (dlr + dgm_ref[:, :DH].astype(F32)).astype(BF16)
        dp_ref[:, LR0 + DH:GATE_GLA0] = dgm_ref[:, DH:D]
        dp_ref[:, GATE_GLA0:GATE_GLA0 + DH] = dgm_ref[:, D:GW] + dgm_ref[:, GW:GW + DH]
        dp_ref[:, GATE_GLA0 + DH:GATE_GLA0 + GW] = dgm_ref[:, GW + DH:]
        dp_ref[:, GATE_GLA0 + GW:] = jnp.zeros((TR, W_IN_COLS - GATE_GLA0 - GW), BF16)

    return pl.pallas_call(
        body, name="gates_bwd", grid=(t // TR,),
        in_specs=[seg(0), seg(2), seg(3), _rowcol(DH, LR0 // DH), _full((2, 2, HW)), _full((DH, D)), _full((1, D)),
                  _row(2 * GW), _row(D)] + [_row(D)] * 8,
        out_specs=[_row(W_IN_COLS), _full((8, HW)), _full((DH, D)), _full((8, D))],
        out_shape=[jax.ShapeDtypeStruct((t, W_IN_COLS), BF16), jax.ShapeDtypeStruct((8, HW), F32),
                   jax.ShapeDtypeStruct((DH, D), F32), jax.ShapeDtypeStruct((8, D), F32)],
        compiler_params=_cp("arbitrary"),
    )(p, p, p, p, hg_lb, wgk, bgk, dgm, dgo, dq_f, dq_b, dv_f, dv_b, dk_f, dk_b, dg_f, dg_b)


def _scan_consts(rev):
    r = lax.broadcasted_iota(jnp.int32, (CHUNK, CHUNK), 0)
    u = lax.broadcasted_iota(jnp.int32, (CHUNK, CHUNK), 1)
    rp = lax.broadcasted_iota(jnp.int32, (CHUNK, 1), 0)
    if rev:
        r, u, rp = CHUNK - 1 - r, CHUNK - 1 - u, CHUNK - 1 - rp
    tri = jnp.where(u <= r, 1.0, 0.0).astype(F32)
    tri_t = jnp.where(r <= u, 1.0, 0.0).astype(F32)
    lv = []
    for b in LEVELS:
        sh = b.bit_length() - 1
        pair = ((r >> sh) == (u >> sh) + 1) & (((u >> sh) & 1) == 0)
        pair_t = ((u >> sh) == (r >> sh) + 1) & (((r >> sh) & 1) == 0)
        tside = ((rp >> sh) & 1) == 1
        lv.append((pair, pair_t, tside, jnp.where(tside, 1.0, -1.0).astype(F32)))
    bd = LEVELS[-1].bit_length() - 1
    diag = ((r >> bd) == (u >> bd)) & (u <= r)
    diag_t = ((r >> bd) == (u >> bd)) & (r <= u)
    return tri, tri_t, lv, diag, diag_t


def _row_of(pos, rev):
    return CHUNK - 1 - pos if rev else pos


def _chunk_terms(cum, b_scr, consts, rev):
    _, _, lv, _, _ = consts
    terms = []
    for b, (_, _, _, sgn) in zip(LEVELS, lv):
        pieces = []
        for j in range(CHUNK // (2 * b)):
            row = _row_of(2 * b * j + b - 1, rev)
            pieces.append(jnp.broadcast_to(b_scr[row:row + 1, :], (2 * b, DH)))
        if rev:
            pieces = pieces[::-1]
        bnd = pieces[0] if len(pieces) == 1 else jnp.concatenate(pieces, axis=0)
        terms.append(jnp.exp((cum - bnd) * sgn))
    b = LEVELS[-1]
    pieces = []
    for j in range(CHUNK // b):
        if j == 0:
            pieces.append(jnp.zeros((b, DH), F32))
        else:
            row = _row_of(b * j - 1, rev)
            pieces.append(jnp.broadcast_to(b_scr[row:row + 1, :], (b, DH)))
    if rev:
        pieces = pieces[::-1]
    start = jnp.concatenate(pieces, axis=0)
    wq = jnp.exp(jnp.minimum(cum - start, 0.0))
    wk = jnp.exp(jnp.minimum(start - cum, EXP_CLAMP))
    terms.append((wq, wk))
    return terms


def _run_staged(units):
    live = list(units)
    while live:
        nxt = []
        for u in live:
            try:
                next(u)
                nxt.append(u)
            except StopIteration:
                pass
        live = nxt


SCAN_TB = 256
SCAN_CB = SCAN_TB // CHUNK


def _block_order(i, ntb, rev):
    nctx = CTX // SCAN_TB
    if not rev:
        return i
    return jnp.where(i < nctx, nctx - 1 - i, ntb - 1 - (i - nctx))


def _chunk_in_block(j, rev):
    return SCAN_CB - 1 - j if rev else j


def _scan_fwd(q, k, v, g, rev):
    t = q.shape[0]
    nc = t // CHUNK
    hpb = SCAN_HEADS_FWD

    def body(q_ref, k_ref, v_ref, g_ref, o_ref, st_ref, s_scr, b_scr):
        consts = _scan_consts(rev)
        _, _, lv, diag, _ = consts
        masks = [lvl[0] for lvl in lv] + [diag]

        @pl.when(pl.program_id(1) == 0)
        def _():
            s_scr[...] = jnp.zeros_like(s_scr)

        tri = consts[0]
        state = {hh: s_scr[hh] for hh in range(hpb)}

        def unit(hh, j):
            sl = slice(hh * DH, (hh + 1) * DH)
            c = _chunk_in_block(j, rev)
            rows = slice(c * CHUNK, (c + 1) * CHUNK)
            b_ref = b_scr.at[hh * SCAN_CB + j]
            qc, kc, vc, gc = q_ref[rows, sl], k_ref[rows, sl], v_ref[rows, sl], g_ref[rows, sl]
            cum = _split_dot(tri, gc)
            b_ref[...] = cum
            yield
            terms = _chunk_terms(cum, b_ref, consts, rev)
            qf, kf = qc.astype(F32), kc.astype(F32)
            xs = [(jnp.where(tside, qf, kf) * w).astype(BF16) for w, (_, _, tside, _) in zip(terms[:-1], lv)]
            qd, kd = (qf * terms[-1][0]).astype(BF16), (kf * terms[-1][1]).astype(BF16)
            tot = _colsum(gc)
            qe = (qf * jnp.exp(cum)).astype(BF16)
            ke = (kf * jnp.exp(tot - cum)).astype(BF16)
            vb = vc.astype(BF16)
            yield
            scs = [_dot(x, x, NT) for x in xs] + [_dot(qd, kd, NT)]
            kv = _dot(vb, ke, TN)
            yield
            a = jnp.zeros((CHUNK, CHUNK), F32)
            for sc, m in zip(scs, masks):
                a = a + jnp.where(m, sc, 0.0)
            o_intra = _dot(a.astype(BF16), vb, NN)
            yield
            st = state[hh]
            st_ref[hh, c] = st
            o_ref[rows, sl] = o_intra + _dot(qe, st.astype(BF16), NT)
            state[hh] = st * jnp.exp(tot) + kv
            yield

        _run_staged([unit(hh, j) for hh in range(hpb) for j in range(SCAN_CB)])
        for hh in range(hpb):
            s_scr[hh] = state[hh]

    ntb = t // SCAN_TB
    col = pl.BlockSpec((SCAN_TB, hpb * DH), lambda h, i: (_block_order(i, ntb, rev), h))
    return pl.pallas_call(
        body, name="scan_fwd_" + ("bw" if rev else "fw"), grid=(NH // hpb, ntb),
        in_specs=[col] * 4,
        out_specs=[col, pl.BlockSpec((hpb, SCAN_CB, DH, DH), lambda h, i: (h, _block_order(i, ntb, rev), 0, 0))],
        out_shape=[jax.ShapeDtypeStruct((t, D), F32), jax.ShapeDtypeStruct((NH, nc, DH, DH), F32)],
        scratch_shapes=[pltpu.VMEM((hpb, DH, DH), F32), pltpu.VMEM((hpb * SCAN_CB, CHUNK, DH), F32)],
        compiler_params=_cp("parallel", "arbitrary"),
    )(q, k, v, g)


def _scan_bwd(q, k, v, g, do, states, rev):
    t = q.shape[0]
    nc = t // CHUNK
    hpb = SCAN_HEADS_BWD

    def body(q_ref, k_ref, v_ref, g_ref, do_ref, st_ref, dq_ref, dk_ref, dv_ref, dg_ref, ds_scr, b_scr):
        consts = _scan_consts(rev)
        _, tri_t, lv, diag, diag_t = consts
        masks = [(lvl[0], lvl[1]) for lvl in lv] + [(diag, diag_t)]
        @pl.when(pl.program_id(1) == 0)
        def _():
            ds_scr[...] = jnp.zeros_like(ds_scr)

        tri = consts[0]
        dstate = {hh: ds_scr[hh] for hh in range(hpb)}

        def unit(hh, jj):
            sl = slice(hh * DH, (hh + 1) * DH)
            c = _chunk_in_block(SCAN_CB - 1 - jj, rev)
            rows = slice(c * CHUNK, (c + 1) * CHUNK)
            b_ref = b_scr.at[hh * SCAN_CB + jj]
            qc, kc, vc, gc = q_ref[rows, sl], k_ref[rows, sl], v_ref[rows, sl], g_ref[rows, sl]
            dob = do_ref[rows, sl].astype(BF16)
            vb = vc.astype(BF16)
            cum = _split_dot(tri, gc)
            b_ref[...] = cum
            da = _dot(dob, vb, NT)
            da_t = _dot(vb, dob, NT)
            yield
            terms = _chunk_terms(cum, b_ref, consts, rev)
            qf, kf = qc.astype(F32), kc.astype(F32)
            xs = [(jnp.where(tside, qf, kf) * w).astype(BF16) for w, (_, _, tside, _) in zip(terms[:-1], lv)]
            wqd, wkd = terms[-1]
            qdb, kdb = (qf * wqd).astype(BF16), (kf * wkd).astype(BF16)
            tot = _colsum(gc)
            e_tot = jnp.exp(tot)
            e_b = jnp.exp(cum)
            e_t = jnp.exp(tot - cum)
            qeb = (qf * e_b).astype(BF16)
            keb = (kf * e_t).astype(BF16)
            dsym = [(jnp.where(m, da, 0.0) + jnp.where(m_t, da_t, 0.0)).astype(BF16) for m, m_t in masks[:-1]]
            dad = (jnp.where(diag, da, 0.0).astype(BF16), jnp.where(diag_t, da_t, 0.0).astype(BF16))
            yield
            sym = [_dot(x, x, NT) for x in xs]
            dxs = [_dot(d, x, NN) for d, x in zip(dsym, xs)]
            at_d = _dot(kdb, qdb, NT)
            dqt_d = _dot(dad[0], kdb, NN)
            dkt_d = _dot(dad[1], qdb, NN)
            qd = _dot(dob, qeb, TN)
            yield
            a_t = jnp.where(diag_t, at_d, 0.0)
            dq = dqt_d * wqd
            dk = dkt_d * wkd
            db = dqt_d * qdb.astype(F32) - dkt_d * kdb.astype(F32)
            for s, dx, x, w, (_, m_t, tside, sgn) in zip(sym, dxs, xs, terms[:-1], lv):
                a_t = a_t + jnp.where(m_t, s, 0.0)
                dxw = dx * w
                dq = dq + jnp.where(tside, dxw, 0.0)
                dk = dk + jnp.where(tside, 0.0, dxw)
                db = db + (dx * x.astype(F32)) * sgn
            dv_intra = _dot(a_t.astype(BF16), dob, NN)
            st = st_ref[hh, c]
            stb = st.astype(BF16)
            dqe = _dot(dob, stb, NN)
            yield
            dst = dstate[hh]
            dstb = dst.astype(BF16)
            dstate[hh] = dst * e_tot + qd
            dv_ref[rows, sl] = (dv_intra + _dot(keb, dstb, NT)).astype(BF16)
            dke = _dot(vb, dstb, NN)
            yield
            qe = qeb.astype(F32)
            ke = keb.astype(F32)
            dq_ref[rows, sl] = (dq + dqe * e_b).astype(BF16)
            dk_ref[rows, sl] = (dk + dke * e_t).astype(BF16)
            db = db + dqe * qe - dke * ke
            dtot = _colsum(dstb.astype(F32) * stb.astype(F32)) * e_tot + _colsum(dke * ke)
            dg_ref[rows, sl] = _split_dot(tri_t, db) + dtot
            yield

        _run_staged([unit(hh, jj) for hh in range(hpb) for jj in range(SCAN_CB)])
        for hh in range(hpb):
            ds_scr[hh] = dstate[hh]

    ntb = t // SCAN_TB
    blk = lambda i: _block_order(ntb - 1 - i, ntb, rev)
    col = pl.BlockSpec((SCAN_TB, hpb * DH), lambda h, i: (blk(i), h))
    out = jax.ShapeDtypeStruct((t, D), F32)
    outb = jax.ShapeDtypeStruct((t, D), BF16)
    return pl.pallas_call(
        body, name="scan_bwd_" + ("bw" if rev else "fw"), grid=(NH // hpb, ntb),
        in_specs=[col] * 5 + [pl.BlockSpec((hpb, SCAN_CB, DH, DH), lambda h, i: (h, blk(i), 0, 0))],
        out_specs=[col] * 4,
        out_shape=[outb] * 3 + [out],
        scratch_shapes=[pltpu.VMEM((hpb, DH, DH), F32), pltpu.VMEM((hpb * SCAN_CB, CHUNK, DH), F32)],
        compiler_params=_cp("parallel", "arbitrary"),
    )(q, k, v, g, do, states)


W_IN_GRAD_CHUNKS = (("a", (0, 512)), ("b", (0, 256)), ("b", (256, 512)))
W_IN_REF = 6688
W_IN_PAD = 896
W_IN_PIECE = 256
W_IN_STAGES = (3, 4)


def _assemble_w_in(g, rows, prev, name):
    n, r, wp = g.shape
    tr = W_IN_PIECE
    tiles = wp // DH
    first = rows[0] // tr

    def body(g_ref, *refs):
        o_ref = refs[-1]
        lane = lax.broadcasted_iota(jnp.int32, (tr, DH), 1)
        for t in range(W_IN_COLS // DH):
            acc = None
            for j in range(n):
                c = DH * t - W_IN_SHARD * j
                if c <= -DH or c >= W_IN_SHARD:
                    continue
                k, s = divmod(c, DH)
                lo = g_ref[j, :, k * DH:(k + 1) * DH] if 0 <= k < tiles else None
                hi = g_ref[j, :, (k + 1) * DH:(k + 2) * DH] if s and 0 <= k + 1 < tiles else None
                if s:
                    zero = jnp.zeros((tr, DH), g.dtype)
                    lo = zero if lo is None else pltpu.roll(lo, DH - s, 1)
                    hi = zero if hi is None else pltpu.roll(hi, DH - s, 1)
                    part = jnp.where(lane < DH - s, lo, hi)
                else:
                    part = lo
                acc = part if acc is None else acc + part
            o_ref[:, t * DH:(t + 1) * DH] = jnp.zeros((tr, DH), g.dtype) if acc is None else acc

    held = [] if prev is None else [prev]
    return pl.pallas_call(
        body, name=name, grid=((rows[1] - rows[0]) // tr,),
        in_specs=[pl.BlockSpec((n, tr, wp), lambda i: (0, first + i, 0))] + [pl.BlockSpec(memory_space=pl.ANY)] * len(held),
        out_specs=pl.BlockSpec((tr, W_IN_COLS), lambda i: (first + i, 0)),
        out_shape=jax.ShapeDtypeStruct((r, W_IN_COLS), g.dtype),
        input_output_aliases={1: 0} if held else {},
        compiler_params=_cp("parallel"),
    )(g, *held)


def _gate_cols(w):
    return jnp.pad(w, ((0, 0), (GOFF, GW - GOFF - D)))


def _gate_rows(w):
    return jnp.pad(w, ((GOFF, GW - GOFF - D), (0, 0)))


def _layout_wgk(w):
    r = w.shape[1]
    top = jnp.concatenate([w[0], jnp.zeros_like(w[0])], axis=1)
    bot = jnp.concatenate([jnp.zeros_like(w[1]), w[1]], axis=1)
    return jnp.concatenate([top, bot, jnp.zeros((DH - 2 * r, D), w.dtype)], axis=0)


def _unlayout_wgk(d, r=16):
    return jnp.stack([d[:r, :HW], d[r:2 * r, HW:]])


def _local_step(z, target, modc, modx, norms, onw, hg_lb, wgk, bgk, get_w_in, get_mix, get_ffn, send):
    n_pre1, n_post1, n_pre2, n_post2 = norms
    t = z[0].shape[0] + z[1].shape[0]
    tm = 1152 if t % 1152 == 0 else 256
    h1 = _prenorm(z, n_pre1, modc, modx, 0, 1, "prenorm1")
    w_in = get_w_in(h1)
    p = _matmul(h1, w_in, NN, BF16, "mm_in", t, 1024, D)
    q, v, k_f, k_b, g_f, g_b = _gates_fwd(p, hg_lb, wgk, bgk)
    o_f, st_f = _scan_fwd(q, k_f, v, g_f, False)
    o_b, st_b = _scan_fwd(q, k_b, v, g_b, True)
    y = _post_fwd(o_f, o_b, p, onw)
    w_br_hg, w_br_gla, w_out = get_mix(y)
    u1, u2, merged = _branch_merge(y, w_br_hg, w_br_gla, p)
    y1 = _matmul(merged, w_out, NN, BF16, "mm_out", tm, 512, GW)
    z1, h2 = _mid_fwd(z, y1, n_post1, n_pre2, modc, modx)
    w_gu_t, w_down = get_ffn(h2)
    u, v_ff, act = _mm_gu_act(h2, w_gu_t[0], w_gu_t[1], "mm_gu", tm)
    y2 = _matmul(act, w_down, NN, BF16, "mm_down", t, 512, D_FF)
    dz, dy2, loss_vec, sm_final = _final(z1, y2, target, n_post2, modc, modx)
    du, dv_ff = _mm_down_dx_act(dy2, w_down, u, v_ff, "mm_down_dx", tm)
    d_w_down = _matmul(act, dy2, TN, BF16, "mm_down_dw", D_FF // 2, 1024, t)
    dh2 = _matmul((du, dv_ff), w_gu_t, NN, BF16, "mm_gu_dx", tm, 512, D_FF)
    d_w_gate_t = _matmul(du, h2, TN, BF16, "mm_gate_dw", D_FF // 2, 1024, t)
    d_w_up_t = _matmul(dv_ff, h2, TN, BF16, "mm_up_dw", D_FF // 2, 1024, t)
    dh2 = send(("w_down", "w_gate_t", "w_up_t"), (d_w_down, d_w_gate_t, d_w_up_t), dh2)
    dz, dy1, sm_mid = _mid_bwd(dh2, dz, z1, y1, n_post1, n_pre2, modc, modx)
    dmerged = _matmul(dy1, w_out, NT, BF16, "mm_out_dx", tm, GW, D)
    d_w_out = _matmul(merged, dy1, TN, BF16, "mm_out_dw", GW, 512, t)
    du1, du2, dgm, dy_hg, dy_gla = _branch_merge_bwd(dmerged, p, u1, u2, w_br_hg, w_br_gla)
    d_w_br_hg = _matmul(y, du1, TN, BF16, "mm_br_hg_dw", HW, GW, t, a_off=0, m_out=HW)
    d_w_br_gla = _matmul(y, du2, TN, BF16, "mm_br_gla_dw", HW, GW, t, a_off=1, m_out=HW)
    dy_hg = send(("w_out", "w_br_hg", "w_br_gla"), (d_w_out, d_w_br_hg, d_w_br_gla), dy_hg)
    do, dgo, sm_post = _post_bwd(dy_hg, dy_gla, o_f, o_b, p, onw)
    dq_f, dk_f, dv_f, dg_f = _scan_bwd(q, k_f, v, g_f, do, st_f, False)
    dq_b, dk_b, dv_b, dg_b = _scan_bwd(q, k_b, v, g_b, do, st_b, True)
    dp, d_lb, d_wgk, d_bgk = _gates_bwd(p, hg_lb, wgk, bgk, dgm, dgo, dq_f, dq_b, dv_f, dv_b, dk_f, dk_b, dg_f, dg_b)
    d_w_in_a = _matmul(h1, dp, TN, BF16, "mm_in_dw_a", 512, 1024, t, a_off=0, m_out=D // 2)
    dp = send(("w_in_a",), (d_w_in_a,), dp)
    d_w_in_b = _matmul(h1, dp, TN, BF16, "mm_in_dw_b", 512, 1024, t, a_off=1, m_out=D // 2)
    dp = send(("w_in_b",), (d_w_in_b,), dp)
    dh1 = _matmul(dp, w_in, NT, BF16, "mm_in_dx", tm, 512, W_IN_COLS // 2)
    grad_x, sm_pre = _pre_bwd(dh1, dz, z, n_pre1, modc, modx)
    return dict(loss_vec=loss_vec, grad_x=grad_x, sm_final=sm_final, sm_mid=sm_mid, sm_post=sm_post, sm_pre=sm_pre,
                d_lb=d_lb, d_wgk=d_wgk, d_bgk=d_bgk)


MESH = pl.DeviceIdType.MESH
ANY = pl.BlockSpec(memory_space=pl.ANY)
N_REL = N_DEV - 1


def _place():
    return lax.axis_index("x"), lax.axis_index("y"), lax.axis_index("c")


def _slot(p):
    return 4 * p[0] + 2 * p[1] + p[2]


HBM = pl.BlockSpec(memory_space=pltpu.HBM)
SEM = pl.BlockSpec(memory_space=pltpu.SEMAPHORE)
EFFECT = pltpu.SideEffectType.DATAFLOW_SIDE_EFFECTING


def _peer_of(x, y, c, k):
    flip = lambda v, bit: 1 - v if bit else v
    return flip(x, k & 4), flip(y, k & 2), flip(c, k & 1)


def _view_whole(src, slot):
    return src


def _view_near(src, slot):
    return src


_view_near.peers = (1, 2, 4, 6)


def _view_near_rows(rows):
    def view(src, slot):
        return src.at[pl.ds(rows[0], rows[1] - rows[0])]
    view.peers = _view_near.peers
    view.land = lambda land, slot: land.at[slot, pl.ds(rows[0], rows[1] - rows[0])]
    return view


def _view_block(src, slot):
    return src.at[slot]


W_IN_SHARD = W_IN_REF // N_DEV


def _view_window(rows):
    def view(src, slot):
        col0 = pl.multiple_of((W_IN_SHARD * slot // DH) * DH, DH)
        return src.at[pl.ds(rows[0], rows[1] - rows[0]), pl.ds(col0, D)]
    return view


def _split_copies(view, srcs, lands, send_sems, recv_sems, local_sems):
    x, y, c = _place()
    me = _slot((x, y, c))
    into = getattr(view, "land", lambda land, slot: land.at[slot])
    local, sends, waits = [], [], []
    for a, (src, land) in enumerate(zip(srcs, lands)):
        local.append(pltpu.make_async_copy(view(src, me), into(land, me), local_sems.at[a]))
        for k in getattr(view, "peers", range(1, N_DEV)):
            peer = _peer_of(x, y, c, k)
            mine = view(src, _slot(peer))
            sems = dict(send_sem=send_sems.at[N_REL * a + k - 1], recv_sem=recv_sems.at[N_REL * a + k - 1],
                        device_id=peer, device_id_type=MESH)
            sends.append(pltpu.make_async_remote_copy(src_ref=mine, dst_ref=into(land, me), **sems))
            waits.append(pltpu.make_async_remote_copy(src_ref=mine, dst_ref=into(land, _slot(peer)), **sems))
    return local, sends, waits


def _split_start(groups, name, after):
    built = []
    for view, srcs, lands in groups:
        lands = [lax.empty(l, s.dtype) if isinstance(l, tuple) else l for l, s in zip(lands, srcs)]
        built.append((view, list(srcs), lands))
    bufs = [b for _, srcs, lands in built for b in srcs + lands]
    nb, ng = len(bufs), len(built)

    def body(*refs):
        buf_refs, sem_refs, token = refs[:nb], refs[nb + 1:nb + 1 + 3 * ng], refs[-1]
        pos = 0
        for i, (view, srcs, _) in enumerate(built):
            n = len(srcs)
            local, sends, _ = _split_copies(view, buf_refs[pos:pos + n], buf_refs[pos + n:pos + 2 * n],
                                            *sem_refs[3 * i:3 * i + 3])
            pos += 2 * n
            for cp in local + sends:
                cp.start()
        token[...] = jnp.zeros_like(token)

    sems = []
    for _, srcs, _ in built:
        n = len(srcs)
        sems += [pltpu.SemaphoreType.DMA((N_REL * n,)), pltpu.SemaphoreType.DMA((N_REL * n,)),
                 pltpu.SemaphoreType.DMA((n,))]
    hbm = lambda a: pltpu.with_memory_space_constraint(a, pltpu.HBM)
    out = pl.pallas_call(
        body, name=name,
        out_shape=(*sems, *[pltpu.HBM(b.shape, b.dtype) for b in bufs], jax.ShapeDtypeStruct((8, DH), F32)),
        in_specs=[HBM] * nb + [ANY],
        out_specs=(*([SEM] * (3 * ng)), *([HBM] * nb), pl.BlockSpec(memory_space=pltpu.VMEM)),
        input_output_aliases={i: 3 * ng + i for i in range(nb)},
        compiler_params=pltpu.CompilerParams(has_side_effects=EFFECT),
    )(*[hbm(b) for b in bufs], after)
    handles, pos = [], 3 * ng
    for i, (view, srcs, _) in enumerate(built):
        n = len(srcs)
        handles.append(dict(view=view, n=n, sems=out[3 * i:3 * i + 3], srcs=list(out[pos:pos + n]),
                            lands=list(out[pos + n:pos + 2 * n])))
        pos += 2 * n
    return handles, out[-1]


def _split_wait(handle, name, after, srcs=None, lands=None):
    view, n, sems = handle["view"], handle["n"], handle["sems"]
    srcs = handle["srcs"] if srcs is None else srcs
    lands = handle["lands"] if lands is None else lands
    afters = list(after) if isinstance(after, (list, tuple)) else [after]

    def body(*refs):
        src_refs, land_refs = refs[:n], refs[n:2 * n]
        send_sems, recv_sems, local_sems = refs[2 * n:2 * n + 3]
        local, _, waits = _split_copies(view, src_refs, land_refs, send_sems, recv_sems, local_sems)
        for cp in waits:
            cp.wait_send()
            cp.wait_recv()
        for cp in local:
            cp.wait()

    out = pl.pallas_call(
        body, name=name,
        out_shape=(*[pltpu.HBM(s.shape, s.dtype) for s in srcs], *[pltpu.HBM(l.shape, l.dtype) for l in lands]),
        in_specs=[HBM] * (2 * n) + [SEM, SEM, SEM] + [ANY] * len(afters),
        out_specs=tuple([HBM] * (2 * n)),
        input_output_aliases={i: i for i in range(2 * n)},
        compiler_params=pltpu.CompilerParams(has_side_effects=EFFECT),
    )(*srcs, *lands, *sems, *afters)
    handle["srcs"] = list(out[:n])
    return list(out[n:])


def _tie(x, token, name):
    def body(x_ref, t_ref, o_ref):
        pass

    return pl.pallas_call(
        body, name=name, out_shape=jax.ShapeDtypeStruct(x.shape, x.dtype),
        in_specs=[ANY, ANY], out_specs=ANY, input_output_aliases={0: 0},
    )(x, token)


def _forward_to_sibling(land, name, rows):
    def body(land_ref, out_ref, send_sems, recv_sems):
        x, y, c = _place()
        sibling = (x, y, 1 - c)
        chips = [(1 - x, y), (x, 1 - y), (1 - x, 1 - y)]
        piece = pl.ds(rows[0], rows[1] - rows[0])

        def copy(j, core):
            blk = _slot((*chips[j], core))
            return pltpu.make_async_remote_copy(src_ref=land_ref.at[blk, piece], dst_ref=out_ref.at[blk, piece],
                                                send_sem=send_sems.at[j], recv_sem=recv_sems.at[j],
                                                device_id=sibling, device_id_type=MESH)

        sends = [copy(j, c) for j in range(3)]
        for cp in sends:
            cp.start()
        for j in range(3):
            copy(j, 1 - c).wait_recv()
        for cp in sends:
            cp.wait_send()

    return pl.pallas_call(
        body, name=name, in_specs=[ANY], out_specs=ANY, input_output_aliases={0: 0},
        out_shape=jax.ShapeDtypeStruct(land.shape, land.dtype),
        scratch_shapes=[pltpu.SemaphoreType.DMA((3,)), pltpu.SemaphoreType.DMA((3,))],
    )(land)


def _mod_fwd(a, w, b):
    def body(a_ref, w_ref, b_ref, o_ref):
        o_ref[...] = _dot(_silu(a_ref[...]), w_ref[...], NN, precision=HI) + b_ref[...]

    return pl.pallas_call(
        body, name="mod_fwd", out_shape=jax.ShapeDtypeStruct((a.shape[0], w.shape[1]), F32),
        compiler_params=pltpu.CompilerParams(vmem_limit_bytes=VMEM_LIMIT),
    )(a, w, b)


def _mod_bwd(a, d, w):
    def body(a_ref, d_ref, w_ref, dw_ref, dc_ref):
        av = a_ref[...]
        dv = d_ref[...]
        dw_ref[...] = _dot(_silu(av), dv, TN, precision=HI)
        da = _dot(dv[0:8, :], w_ref[...], NT, precision=HI) * _dsilu(av[0:8, :])
        row = lax.broadcasted_iota(jnp.int32, da.shape, 0)
        dc_ref[...] = jnp.where(row == 0, da, 0.0)

    return pl.pallas_call(
        body, name="mod_bwd",
        out_shape=[jax.ShapeDtypeStruct(w.shape, F32), jax.ShapeDtypeStruct((8, w.shape[0]), F32)],
        compiler_params=pltpu.CompilerParams(vmem_limit_bytes=VMEM_LIMIT),
    )(a, d, w)


def _sum_devices(g):
    def body(g_ref, o_ref):
        acc = g_ref[0]
        for i in range(1, g.shape[0]):
            acc = acc + g_ref[i]
        o_ref[...] = acc

    return pl.pallas_call(body, name="sum_devices_%d" % g.shape[1],
                          out_shape=jax.ShapeDtypeStruct(g.shape[1:], F32))(g)


def _sum_windows(g, name):
    n, r, c = g.shape
    tr = 128

    def body(g_ref, o_ref):
        x, y, cc = _place()
        lane0 = (W_IN_SHARD * _slot((x, y, cc))) % DH
        acc = g_ref[0].astype(F32)
        for i in range(1, n):
            acc = acc + g_ref[i].astype(F32)
        o_ref[...] = pltpu.roll(acc, (c - lane0) % c, 1).T

    return pl.pallas_call(
        body, name=name, grid=(r // tr,),
        in_specs=[pl.BlockSpec((n, tr, c), lambda i: (0, i, 0))],
        out_specs=pl.BlockSpec((c, tr), lambda i: (0, i)),
        out_shape=jax.ShapeDtypeStruct((c, r), F32),
        compiler_params=_cp("parallel"),
    )(g)


def _adam_rows(r, c, n):
    budget = 10 * 1024 * 1024
    best = None
    for tr in range(16, r + 1, 16):
        if r % tr == 0 and tr * c * (2 * n + 28) <= budget:
            best = tr
    return best if best is not None else r


def _adamw(g, w, m, v, name):
    n, r, c = g.shape
    tr = _adam_rows(r, c, n)
    bc1 = 1.0 - ADAM_B1 ** ADAM_STEP
    bc2 = 1.0 - ADAM_B2 ** ADAM_STEP

    def body(g_ref, w_ref, m_ref, v_ref, go_ref, d_ref, mo_ref, vo_ref):
        grad = g_ref[0].astype(F32)
        for i in range(1, n):
            grad = grad + g_ref[i].astype(F32)
        go_ref[...] = grad
        m_new = ADAM_B1 * m_ref[...] + (1.0 - ADAM_B1) * grad
        v_new = ADAM_B2 * v_ref[...] + (1.0 - ADAM_B2) * (grad * grad)
        mo_ref[...] = m_new
        vo_ref[...] = v_new
        d_ref[...] = -ADAM_LR * ((m_new / bc1) / (jnp.sqrt(v_new / bc2) + ADAM_EPS) + ADAM_WD * w_ref[...])

    blk = pl.BlockSpec((tr, c), lambda i: (i, 0))
    out = jax.ShapeDtypeStruct((r, c), F32)
    return pl.pallas_call(
        body, name=name, grid=(r // tr,),
        in_specs=[pl.BlockSpec((n, tr, c), lambda i: (0, i, 0)), blk, blk, blk],
        out_specs=[blk] * 4, out_shape=[out] * 4,
        compiler_params=_cp("parallel"),
    )(g, w, m, v)


ADAM_ROWS3 = 168


def _adam_math(grad, w, m, v):
    bc1 = 1.0 - ADAM_B1 ** ADAM_STEP
    bc2 = 1.0 - ADAM_B2 ** ADAM_STEP
    m_new = ADAM_B1 * m + (1.0 - ADAM_B1) * grad
    v_new = ADAM_B2 * v + (1.0 - ADAM_B2) * (grad * grad)
    delta = -ADAM_LR * ((m_new / bc1) / (jnp.sqrt(v_new / bc2) + ADAM_EPS) + ADAM_WD * w)
    return delta, m_new, v_new


def _adamw_rows3(g, w3, m3, v3, name, cols, prev):
    r, _, _ = w3.shape
    c = cols[1] - cols[0]
    n = min(-(-r // 16) * 8, ADAM_ROWS3 * D // c // 8 * 8)
    starts = list(range(0, r - n, n)) + [r - n]
    held = [] if prev is None else list(prev)

    def body(g_hbm, w_hbm, m_hbm, v_hbm, *refs):
        go_hbm, d_hbm, mo_hbm, vo_hbm, gbuf, ibuf, obuf, in_sems, out_sems = refs[len(held):]
        part = lambda h, r0: h.at[pl.ds(r0, n), 0, pl.ds(cols[0], c)]

        def fetch(p):
            r0, slot = starts[p], p % 2
            g0 = (r0 // 8) * 8
            cps = [pltpu.make_async_copy(g_hbm.at[pl.ds(g0, n + 8)], gbuf.at[slot], in_sems.at[slot, 0])]
            cps += [pltpu.make_async_copy(part(h, r0), ibuf.at[slot, k], in_sems.at[slot, 1 + k])
                    for k, h in enumerate((w_hbm, m_hbm, v_hbm))]
            for cp in cps:
                cp.start()
            return cps

        pending, outs = fetch(0), []
        for p, r0 in enumerate(starts):
            slot = p % 2
            nxt = fetch(p + 1) if p + 1 < len(starts) else []
            for cp in pending:
                cp.wait()
            grad = gbuf[slot, pl.ds(r0 - (r0 // 8) * 8, n), :]
            delta, m_new, v_new = _adam_math(grad, ibuf[slot, 0], ibuf[slot, 1], ibuf[slot, 2])
            for cp in outs:
                cp.wait()
            for k, val in enumerate((grad, delta, m_new, v_new)):
                obuf[slot, k] = val
            outs = [pltpu.make_async_copy(obuf.at[slot, k], part(h, r0), out_sems.at[slot, k])
                    for k, h in enumerate((go_hbm, d_hbm, mo_hbm, vo_hbm))]
            for cp in outs:
                cp.start()
            pending = nxt
        for cp in outs:
            cp.wait()

    out = jax.ShapeDtypeStruct(w3.shape, F32)
    return pl.pallas_call(
        body, name=name, in_specs=[ANY] * (4 + len(held)), out_specs=[ANY] * 4, out_shape=[out] * 4,
        input_output_aliases={4 + k: k for k in range(len(held))},
        scratch_shapes=[pltpu.VMEM((2, n + 8, c), F32), pltpu.VMEM((2, 3, n, c), F32), pltpu.VMEM((2, 4, n, c), F32),
                        pltpu.SemaphoreType.DMA((2, 4)), pltpu.SemaphoreType.DMA((2, 4))],
        compiler_params=pltpu.CompilerParams(vmem_limit_bytes=VMEM_LIMIT),
    )(g, w3, m3, v3, *held)


def kernel(x, c, ctx, c_ctx, w_mod, b_mod, norm_pre1, norm_post1, norm_pre2, norm_post2, w_in, hg_lb, hg_onorm, gla_w_gk, gla_b_gk, gla_onorm, w_br_hg, w_br_gla, w_out, w_ff_gate, w_ff_up, w_ff_down, loss_target, m_c_ctx, m_w_mod, m_b_mod, m_norm_pre1, m_norm_post1, m_norm_pre2, m_norm_post2, m_w_in, m_hg_lb, m_hg_onorm, m_gla_w_gk, m_gla_b_gk, m_gla_onorm, m_w_br_hg, m_w_br_gla, m_w_out, m_w_ff_gate, m_w_ff_up, m_w_ff_down, v_c_ctx, v_w_mod, v_b_mod, v_norm_pre1, v_norm_post1, v_norm_pre2, v_norm_post2, v_w_in, v_hg_lb, v_hg_onorm, v_gla_w_gk, v_gla_b_gk, v_gla_onorm, v_w_br_hg, v_w_br_gla, v_w_out, v_w_ff_gate, v_w_ff_up, v_w_ff_down):
    xi, yi, ci = lax.axis_index("x"), lax.axis_index("y"), lax.axis_index("c")
    me = 4 * xi + 2 * yi + ci
    t = CTX + x.shape[1]

    w_in_pieces, w_in_state = [], {}

    def w_in_piece(i):
        return (_view_near_rows((i * W_IN_PIECE, (i + 1) * W_IN_PIECE)), w_in_state["src"], w_in_state["land"])

    def started_w_in(handle):
        w_in_state.update(src=handle["srcs"], land=handle["lands"])
        w_in_pieces.append(handle)

    tr_ = lambda a: jnp.swapaxes(a[0], 0, 1)
    w_in_bf = jnp.pad(w_in[0].astype(BF16), ((0, 0), (0, W_IN_PAD - W_IN_SHARD)))
    w_in_state.update(src=[w_in_bf], land=[lax.empty((N_DEV,) + w_in_bf.shape, BF16)])
    gathered = lambda arrs: [(N_DEV,) + a.shape for a in arrs]
    whole = lambda arrs: (_view_whole, arrs, gathered(arrs))
    small_in = [c, hg_lb, gla_w_gk[0], gla_b_gk[0]]
    (small_handle, piece), tok = _split_start([whole(small_in), w_in_piece(0)], "ag_small_start", c)
    started_w_in(piece)
    c_all, lb_g, wgk_g, bgk_g = _split_wait(small_handle, "ag_small_wait", tok)
    big = [w_in[0], w_br_hg[0], w_br_gla[0], w_out[0], tr_(w_ff_gate), tr_(w_ff_up), w_ff_down[0]]
    big_bf = [None] + [w.astype(BF16) for w in big[1:]]
    cols = lambda g: jnp.transpose(g, (1, 0, 2)).reshape(g.shape[1], N_DEV * g.shape[2])

    def get_w_in(after):
        w_full, first = None, 0
        for s, last in enumerate(W_IN_STAGES):
            for i in range(first, last):
                land = _split_wait(w_in_pieces[i], "ag_w_in_wait%d" % i, after if w_full is None else [after, w_full],
                                   srcs=w_in_state["src"], lands=w_in_state["land"])
                w_in_state.update(src=w_in_pieces[i]["srcs"], land=land)
            rows = (first * W_IN_PIECE, last * W_IN_PIECE)
            w_in_state["land"] = [_forward_to_sibling(w_in_state["land"][0], "ag_w_in_forward%d" % s, rows)]
            w_full = _assemble_w_in(w_in_state["land"][0], rows, w_full, "assemble_w_in%d" % s)
            first = last
        return w_full

    def get_mix(after):
        g_brh, g_brg, g_out = _split_wait(mix_handle, "ag_mix_wait", after)
        return _gate_cols(cols(g_brh)), _gate_cols(cols(g_brg)), _gate_rows(g_out.reshape(D, D))

    def get_ffn(after):
        g_gate, g_up, g_down = _split_wait(ffn_handle, "ag_ffn_wait", after)
        return (g_gate.reshape(D_FF, D), g_up.reshape(D_FF, D)), g_down.reshape(D_FF, D)

    hg_lb_full = jnp.transpose(lb_g, (1, 2, 0, 3)).reshape(2, 2, HW)
    wgk_k = _layout_wgk(jnp.transpose(wgk_g, (1, 2, 0, 3)).reshape(2, 16, HW)).astype(BF16)
    bgk_k = jnp.transpose(bgk_g, (1, 0, 2)).reshape(1, D)
    onw = jnp.concatenate([jnp.tile(hg_onorm, (1, NH // 2)), jnp.tile(gla_onorm, (1, NH // 2))], axis=1)

    n_mod = w_mod.shape[2]
    a9 = jnp.concatenate([c_ctx[None], c_all[:, 0], jnp.zeros((16 - 1 - N_DEV, D), F32)], axis=0)
    b_loc = lax.dynamic_slice(b_mod, (0, me * n_mod), (1, n_mod))
    s_loc = _mod_fwd(a9, w_mod[0], b_loc)
    (mod_handle, piece), tok = _split_start([whole([s_loc]), w_in_piece(1)], "ag_mod_start", s_loc)
    started_w_in(piece)
    for i in range(2, D // W_IN_PIECE):
        (piece,), tok = _split_start([w_in_piece(i)], "ag_w_in_start%d" % i, tok)
        started_w_in(piece)
    s_all, = _split_wait(mod_handle, "ag_mod_wait", tok)
    mod_all = jnp.transpose(s_all, (1, 0, 2)).reshape(16, N_DEV * n_mod)
    pad8 = lambda m: jnp.concatenate([m.reshape(6, D), jnp.zeros((2, D), F32)], axis=0)
    modc = pad8(mod_all[0])
    modx = pad8(lax.dynamic_slice(mod_all, (1 + me, 0), (1, N_DEV * n_mod))[0])

    (mix_handle, ffn_handle), tok = _split_start([whole(big_bf[1:4]), whole(big_bf[4:])], "ag_big_start", s_all)

    z = (ctx[0], x[0])
    modx = _tie(modx, tok, "tie_mod")
    norms = (norm_pre1, norm_post1, norm_pre2, norm_post2)
    shard = lambda d: jnp.transpose(d.reshape(d.shape[0], N_DEV, -1), (1, 0, 2)).astype(BF16)
    rowshard = lambda d: d.reshape(N_DEV, d.shape[0] // N_DEV, d.shape[1]).astype(BF16)
    sent, w_in_grad = [], {}

    def w_in_chunk(i):
        half, rows = W_IN_GRAD_CHUNKS[i]
        return (_view_window(rows), w_in_grad[half], [(N_DEV, rows[1] - rows[0], D)])

    def sent_w_in(i, handle):
        w_in_grad[W_IN_GRAD_CHUNKS[i][0]] = handle["srcs"]
        sent.append(("w_in%d" % i, ["w_in#%d" % i], handle))

    def send(names, grads, x_after):
        if names == ("w_in_a",):
            w_in_grad["a"] = list(grads)
            (handle,), tok = _split_start([w_in_chunk(0)], "grads_w_in0_start", x_after)
            sent_w_in(0, handle)
            return _tie(x_after, tok, "tie_w_in0")
        if names == ("w_in_b",):
            w_in_grad["b"] = list(grads)
            return x_after
        arrs, leaves = [], []
        for nm, g in zip(names, grads):
            if nm in ("w_gate_t", "w_up_t"):
                arrs.append(rowshard(g))
                leaves.append({"w_gate_t": "w_ff_gate", "w_up_t": "w_ff_up"}[nm])
            elif nm == "w_down":
                arrs.append(rowshard(g))
                leaves.append("w_ff_down")
            elif nm == "w_out":
                arrs.append(rowshard(g[GOFF:GOFF + D]))
                leaves.append(nm)
            else:
                arrs.append(shard(g[:, GOFF:GOFF + D]))
                leaves.append(nm)
        (handle,), tok = _split_start([(_view_block, arrs, [a.shape for a in arrs])], "grads_%s_start" % names[0],
                                      x_after)
        sent.append((names[0], leaves, handle))
        return _tie(x_after, tok, "tie_" + names[0])

    r = _local_step(z, loss_target[0], modc, modx, norms, onw, hg_lb_full, wgk_k, bgk_k,
                    get_w_in, get_mix, get_ffn, send)
    grad_x = r["grad_x"][None]

    sm_pre, sm_mid, sm_fin = r["sm_pre"], r["sm_mid"], r["sm_final"]
    dmodc = jnp.stack([sm_pre[0], sm_pre[2], sm_mid[4], sm_mid[0], sm_mid[2], sm_fin[0]]).reshape(-1)
    dmodx = jnp.stack([sm_pre[1], sm_pre[3], sm_mid[5], sm_mid[1], sm_mid[3], sm_fin[1]]).reshape(-1)
    on = r["sm_post"][0].reshape(NH, DH)
    pieces = [dmodc, dmodx, sm_pre[4], sm_mid[7], sm_mid[6], sm_fin[2], on[:NH // 2].sum(0), on[NH // 2:].sum(0),
              r["d_lb"][:2].reshape(-1), _unlayout_wgk(r["d_wgk"]).reshape(-1), r["d_bgk"][0]]
    loss_local = (0.5 / D) * jnp.sum(r["loss_vec"])
    pieces.append(jnp.concatenate([loss_local.reshape(1), jnp.zeros((DH - 1,), F32)]))
    sizes = [p.shape[0] for p in pieces]
    pack = jnp.concatenate(pieces).reshape(-1, DH)
    moms = [(m_w_in, v_w_in), (m_w_br_hg, v_w_br_hg), (m_w_br_gla, v_w_br_gla), (m_w_out, v_w_out),
            (m_w_ff_gate, v_w_ff_gate), (m_w_ff_up, v_w_ff_up), (m_w_ff_down, v_w_ff_down)]
    names = ["w_in", "w_br_hg", "w_br_gla", "w_out", "w_ff_gate", "w_ff_up", "w_ff_down"]
    wmv = {nm: (w, m, v) for nm, w, (m, v) in zip(names, big, moms)}
    res, updated = {}, {}

    def update(nm):
        w, m, v = wmv[nm]
        if nm in ("w_ff_gate", "w_ff_up"):
            outs = _adamw(recv[nm], w, tr_(m), tr_(v), "adamw_" + nm)
            res[nm] = [jnp.swapaxes(o, 0, 1)[None] for o in outs]
        else:
            outs = _adamw(recv[nm], w, m[0], v[0], "adamw_" + nm)
            res[nm] = [o[None] for o in outs]
        updated[nm] = outs[0]

    (small_handle, handle), tok = _split_start([whole([pack]), w_in_chunk(1)], "small_grads_start", pack)
    sent_w_in(1, handle)
    recv = {}
    for first, leaves, handle in sent:
        if not first.startswith("w_in"):
            recv.update(zip(leaves, _split_wait(handle, "grads_%s_wait" % first, tok)))
    update("w_ff_gate")
    update("w_ff_up")
    pack_all, = _split_wait(small_handle, "small_grads_wait", [updated["w_ff_gate"], updated["w_ff_up"]])
    tot = _sum_devices(pack_all).reshape(-1)
    offs = [sum(sizes[:i]) for i in range(len(sizes))]
    part = lambda i: tot[offs[i]:offs[i] + sizes[i]]
    dmodc_t, dmodx_t = part(0), part(1)
    g_b_mod = (dmodc_t + dmodx_t)[None]
    g_norms = [part(i)[None] for i in (2, 3, 4, 5)]
    g_hg_on, g_gla_on = part(6)[None], part(7)[None]
    lb0 = lax.dynamic_slice(part(8).reshape(2, HW), (0, me * (HW // N_DEV)), (2, HW // N_DEV))
    g_hg_lb = jnp.stack([lb0, -lb0])
    g_wgk = lax.dynamic_slice(part(9).reshape(2, 16, HW), (0, 0, me * (HW // N_DEV)), (2, 16, HW // N_DEV))[None]
    g_bgk = lax.dynamic_slice(part(10).reshape(2, HW), (0, me * (HW // N_DEV)), (2, HW // N_DEV))[None]
    loss = part(11)[0]

    dmx_all = pack_all.reshape(N_DEV, -1)[:, sizes[0]:sizes[0] + sizes[1]]
    d9 = jnp.concatenate([lax.dynamic_slice(dmodc_t[None], (0, me * n_mod), (1, n_mod)),
                          lax.dynamic_slice(dmx_all, (0, me * n_mod), (N_DEV, n_mod)),
                          jnp.zeros((16 - 1 - N_DEV, n_mod), F32)], axis=0)
    g_w_mod, dcc_part = _mod_bwd(a9, d9, w_mod[0])
    (cctx_handle, handle), tok = _split_start([whole([dcc_part]), w_in_chunk(2)], "c_ctx_start", dcc_part)
    sent_w_in(2, handle)
    recv["w_ff_down"] = _tie(recv["w_ff_down"], tok, "tie_down")
    update("w_ff_down")
    res["w_mod"] = [o[None] for o in _adamw(g_w_mod[None], w_mod[0], m_w_mod[0], v_w_mod[0], "adamw_w_mod")]
    for nm in ("w_out", "w_br_hg", "w_br_gla"):
        update(nm)
    dcc_all, = _split_wait(cctx_handle, "c_ctx_wait", [updated["w_ff_down"], res["w_mod"][0]])
    g_c_ctx = _sum_devices(dcc_all)[0]

    small = [("c_ctx", c_ctx, m_c_ctx, v_c_ctx, g_c_ctx), ("b_mod", b_mod, m_b_mod, v_b_mod, g_b_mod),
             ("norm_pre1", norm_pre1, m_norm_pre1, v_norm_pre1, g_norms[0]),
             ("norm_post1", norm_post1, m_norm_post1, v_norm_post1, g_norms[1]),
             ("norm_pre2", norm_pre2, m_norm_pre2, v_norm_pre2, g_norms[2]),
             ("norm_post2", norm_post2, m_norm_post2, v_norm_post2, g_norms[3]),
             ("hg_lb", hg_lb, m_hg_lb, v_hg_lb, g_hg_lb), ("hg_onorm", hg_onorm, m_hg_onorm, v_hg_onorm, g_hg_on),
             ("gla_w_gk", gla_w_gk, m_gla_w_gk, v_gla_w_gk, g_wgk), ("gla_b_gk", gla_b_gk, m_gla_b_gk, v_gla_b_gk, g_bgk),
             ("gla_onorm", gla_onorm, m_gla_onorm, v_gla_onorm, g_gla_on)]
    flat = lambda k: jnp.concatenate([s[k].reshape(-1) for s in small]).reshape(-1, DH)
    outs = _adamw(flat(4)[None], flat(1), flat(2), flat(3), "adamw_small")
    off = 0
    for nm, w, _, _, _ in small:
        res[nm] = [o.reshape(-1)[off:off + w.size].reshape(w.shape) for o in outs]
        off += w.size

    done = [updated[nm] for nm in names[1:]] + [res["w_mod"][0]] + [o for nm, *_ in small for o in res[nm]]
    major = lambda a: jnp.transpose(a, (2, 0, 1))
    outs, row0 = None, 0
    for i, (first, leaves, handle) in enumerate(s for s in sent if s[0].startswith("w_in")):
        half = W_IN_GRAD_CHUNKS[i][0]
        land, = _split_wait(handle, "grads_%s_wait" % first, done, srcs=w_in_grad[half])
        w_in_grad[half] = handle["srcs"]
        rows = (row0, row0 + land.shape[1])
        outs = _adamw_rows3(_sum_windows(land, "sum_windows%d" % i), major(w_in), major(m_w_in), major(v_w_in),
                            "adamw_w_in%d" % i, rows, outs)
        row0 = rows[1]
    res["w_in"] = [jnp.transpose(o, (1, 2, 0)) for o in outs]

    order = ["c_ctx", "w_mod", "b_mod", "norm_pre1", "norm_post1", "norm_pre2", "norm_post2", "w_in", "hg_lb",
             "hg_onorm", "gla_w_gk", "gla_b_gk", "gla_onorm", "w_br_hg", "w_br_gla", "w_out", "w_ff_gate", "w_ff_up",
             "w_ff_down"]
    return (loss, grad_x, *[res[n][k] for k in range(4) for n in order])
```

```python
import functools

import jax
import jax.numpy as jnp
from jax import lax
from jax.experimental import pallas as pl
from jax.experimental.pallas import tpu as pltpu

F32 = jnp.float32
BF16 = jnp.bfloat16
HI = lax.Precision.HIGHEST

N_DEV = 8
D = 1024
CTX = 256
HW = 512
DH = 128
NH = 8
D_FF = 2816
EPS = 1e-6
GLA_NORM = 16.0
CHUNK = 64
TR = 256
NCT = CTX // TR
W_IN_COLS = 7168
MAIN0 = 0
LR0 = 4608
GW = 1152
GOFF = 32
GATE_HG0 = LR0
GATE_GLA0 = LR0 + D
LEVELS = (32, 16, 8)
EXP_CLAMP = 80.0
VMEM_LIMIT = 48 * 1024 * 1024

ADAM_LR, ADAM_B1, ADAM_B2, ADAM_EPS, ADAM_WD, ADAM_STEP = 0.001, 0.9, 0.999, 1e-08, 0.01, 10


def _cp(*sem):
    return pltpu.CompilerParams(dimension_semantics=sem, vmem_limit_bytes=VMEM_LIMIT)


def _sig(x):
    return jax.nn.sigmoid(x)


def _silu(x):
    return x * _sig(x)


def _dsilu(x):
    s = _sig(x)
    return s * (1.0 + x * (1.0 - s))


def _rstd(x):
    return lax.rsqrt(jnp.mean(x * x, axis=-1, keepdims=True) + EPS)


def _rms_bwd(a, y, r):
    return r * (a - y * (r * r) * jnp.mean(a * y, axis=-1, keepdims=True))


def _colsum(x):
    return jnp.sum(x, axis=0, keepdims=True)


def _dot(a, b, dims, precision=None):
    return lax.dot_general(a, b, (dims, ((), ())), preferred_element_type=F32, precision=precision)


NN = ((1,), (0,))
NT = ((1,), (1,))
TN = ((0,), (0,))

SCAN_HEADS_FWD = 4
SCAN_HEADS_BWD = 4


def _split_dot(m, x):
    mb = m.astype(BF16)
    x1 = x.astype(BF16)
    r1 = x - x1.astype(F32)
    x2 = r1.astype(BF16)
    x3 = (r1 - x2.astype(F32)).astype(BF16)
    return _dot(mb, x1, NN) + _dot(mb, x2, NN) + _dot(mb, x3, NN)


def _matmul(a, b, dims, out_dtype, name, tm, tn, tk, a_off=0, m_out=None):
    a_pair = isinstance(a, (tuple, list))
    as_ = list(a) if a_pair else [a]
    a = as_[0]
    pair = isinstance(b, (tuple, list))
    bs = list(b) if pair else [b]
    b1 = bs[0]
    rows = b1.shape[0] * len(bs)
    half = None
    if dims == NN:
        m, k, n = a.shape[0], rows, b1.shape[1]
        a_spec = pl.BlockSpec((tm, tk), lambda i, j, kk: (i, kk + a_off))
        half = b1.shape[0] // tk
        if a_pair:
            assert pair and a.shape[1] == b1.shape[0] and a_off == 0
            a_spec = [pl.BlockSpec((tm, tk), lambda i, j, kk: (i, jnp.minimum(kk, half - 1))),
                      pl.BlockSpec((tm, tk), lambda i, j, kk: (i, jnp.maximum(kk - half, 0)))]
        b_maps = [lambda i, j, kk: (kk, j)] if not pair else [
            lambda i, j, kk: (jnp.minimum(kk, half - 1), j), lambda i, j, kk: (jnp.maximum(kk - half, 0), j)]
        b_specs = [pl.BlockSpec((tk, tn), f) for f in b_maps]
        axis = 2
    elif dims == NT:
        m, k, n = a.shape[0], b1.shape[1], rows
        a_spec = pl.BlockSpec((tm, tk), lambda i, j, kk: (i, kk + a_off))
        half = b1.shape[0] // tn
        b_maps = [lambda i, j, kk: (j, kk)] if not pair else [
            lambda i, j, kk: (jnp.minimum(j, half - 1), kk), lambda i, j, kk: (jnp.maximum(j - half, 0), kk)]
        b_specs = [pl.BlockSpec((tn, tk), f) for f in b_maps]
        axis = 1
    else:
        assert not pair
        m, k = (a.shape[1] if m_out is None else m_out), a.shape[0]
        n = b1.shape[1]
        a_spec = pl.BlockSpec((tk, tm), lambda i, j, kk: (kk, i + a_off))
        b_specs = [pl.BlockSpec((tk, tn), lambda i, j, kk: (kk, j))]
    assert m % tm == 0 and n % tn == 0 and k % tk == 0, (name, m, n, k, tm, tn, tk)
    nk = k // tk
    nb = len(bs)
    na = len(as_)
    assert na == 1 or dims == NN

    def body(*refs):
        a_refs, refs = refs[:na], refs[na:]
        o_ref = refs[nb]
        if pair:
            bv = jnp.where(pl.program_id(axis) < half, refs[0][...], refs[1][...])
        else:
            bv = refs[0][...]
        av = a_refs[0][...] if na == 1 else jnp.where(pl.program_id(2) < half, a_refs[0][...], a_refs[1][...])
        part = _dot(av, bv, dims)
        if nk == 1:
            o_ref[...] = part.astype(o_ref.dtype)
            return
        acc_ref = refs[nb + 1]
        kk = pl.program_id(2)

        @pl.when(kk == 0)
        def _():
            acc_ref[...] = part

        @pl.when(kk > 0)
        def _():
            acc_ref[...] += part

        @pl.when(kk == nk - 1)
        def _():
            o_ref[...] = acc_ref[...].astype(o_ref.dtype)

    return pl.pallas_call(
        body,
        name=name,
        grid=(m // tm, n // tn, nk),
        in_specs=(a_spec if a_pair else [a_spec]) + b_specs,
        out_specs=pl.BlockSpec((tm, tn), lambda i, j, kk: (i, j)),
        out_shape=jax.ShapeDtypeStruct((m, n), out_dtype),
        scratch_shapes=[] if nk == 1 else [pltpu.VMEM((tm, tn), F32)],
        compiler_params=_cp("parallel", "parallel", "arbitrary"),
    )(*as_, *bs)


def _mm_gu_act(h, w_gate_t, w_up_t, name, tm):
    t = h.shape[0]
    tn = D_FF // 2

    def body(a_ref, bg_ref, bu_ref, u_ref, v_ref, act_ref):
        a = a_ref[...]
        u = _dot(a, bg_ref[...], NT)
        v = _dot(a, bu_ref[...], NT)
        u_ref[...] = u.astype(BF16)
        v_ref[...] = v.astype(BF16)
        act_ref[...] = (_silu(u) * v).astype(BF16)

    wspec = pl.BlockSpec((tn, D), lambda i, j: (j, 0))
    ospec = pl.BlockSpec((tm, tn), lambda i, j: (i, j))
    out = jax.ShapeDtypeStruct((t, D_FF), BF16)
    return pl.pallas_call(
        body, name=name, grid=(t // tm, D_FF // tn),
        in_specs=[pl.BlockSpec((tm, D), lambda i, j: (i, 0)), wspec, wspec],
        out_specs=[ospec] * 3, out_shape=[out] * 3,
        compiler_params=_cp("parallel", "parallel"),
    )(h, w_gate_t, w_up_t)


def _mm_down_dx_act(dy, w_down, u, v, name, tm):
    t = dy.shape[0]
    tn = D_FF // 2

    def body(a_ref, b_ref, u_ref, v_ref, du_ref, dv_ref):
        dact = _dot(a_ref[...], b_ref[...], NT)
        u = u_ref[...].astype(F32)
        du_ref[...] = (dact * v_ref[...].astype(F32) * _dsilu(u)).astype(BF16)
        dv_ref[...] = (dact * _silu(u)).astype(BF16)

    ospec = pl.BlockSpec((tm, tn), lambda i, j: (i, j))
    out = jax.ShapeDtypeStruct((t, D_FF), BF16)
    return pl.pallas_call(
        body, name=name, grid=(t // tm, D_FF // tn),
        in_specs=[pl.BlockSpec((tm, D), lambda i, j: (i, 0)), pl.BlockSpec((tn, D), lambda i, j: (j, 0)), ospec, ospec],
        out_specs=[ospec] * 2, out_shape=[out] * 2,
        compiler_params=_cp("parallel", "parallel"),
    )(dy, w_down, u, v)


def _row(c):
    return pl.BlockSpec((TR, c), lambda i: (i, 0))


def _rowcol(width, cb):
    return pl.BlockSpec((TR, width), lambda i: (i, cb))


def _full(shape):
    return pl.BlockSpec(shape, lambda i: (0,) * len(shape))


def _mod_row(mc_ref, mx_ref, k, is_ctx):
    return jnp.where(is_ctx, mc_ref[k:k + 1, :], mx_ref[k:k + 1, :])


def _z_specs():
    return [pl.BlockSpec((TR, D), lambda i: (jnp.minimum(i, NCT - 1), 0)),
            pl.BlockSpec((TR, D), lambda i: (jnp.maximum(i - NCT, 0), 0))]


def _z_tile(c_ref, x_ref, is_ctx):
    return jnp.where(is_ctx, c_ref[...], x_ref[...])


def _acc_row(ref, k, val):
    ref[k:k + 1, :] += val


def _acc_mod(ref, k, is_ctx, val):
    zero = jnp.zeros_like(val)
    ref[k:k + 1, :] += jnp.where(is_ctx, val, zero)
    ref[k + 1:k + 2, :] += jnp.where(is_ctx, zero, val)


def _prenorm(z, nw, modc, modx, i_shift, i_scale, name):
    t = z[0].shape[0] + z[1].shape[0]

    def body(zc_ref, zx_ref, nw_ref, mc_ref, mx_ref, h_ref):
        is_ctx = pl.program_id(0) < NCT
        x = _z_tile(zc_ref, zx_ref, is_ctx)
        n = x * _rstd(x) * nw_ref[...]
        h = n * (1.0 + _mod_row(mc_ref, mx_ref, i_scale, is_ctx)) + _mod_row(mc_ref, mx_ref, i_shift, is_ctx)
        h_ref[...] = h.astype(BF16)

    return pl.pallas_call(
        body, name=name, grid=(t // TR,),
        in_specs=_z_specs() + [_full((1, D)), _full((8, D)), _full((8, D))],
        out_specs=_row(D),
        out_shape=jax.ShapeDtypeStruct((t, D), BF16),
        compiler_params=_cp("parallel"),
    )(*z, nw, modc, modx)


def _hg_lb(lb_ref, d):
    a0 = lb_ref[0, d:d + 1, :]
    a1 = lb_ref[1, d:d + 1, :]
    mx = jnp.maximum(a0, a1)
    e0 = jnp.exp(a0 - mx)
    e1 = jnp.exp(a1 - mx)
    return e0 / (e0 + e1)


def _log_sigmoid(x):
    return jnp.minimum(x, 0.0) - jnp.log(1.0 + jnp.exp(-jnp.abs(x)))


def _gates_fwd(p, hg_lb, wgk, bgk):
    t = p.shape[0]
    seg = lambda j: _rowcol(HW, MAIN0 // HW + j)

    def body(hq_ref, hi_ref, hf_ref, hb_ref, gq_ref, gk_ref, gv_ref, lr_ref, lb_ref, wgk_ref, bgk_ref,
             q_ref, v_ref, kf_ref, kb_ref, gf_ref, gb_ref):
        q_ref[:, :HW] = _silu(hq_ref[...].astype(F32)).astype(BF16)
        q_ref[:, HW:] = (gq_ref[...].astype(F32) * (DH ** -0.5)).astype(BF16)
        v_ref[:, :HW] = hi_ref[...]
        v_ref[:, HW:] = gv_ref[...]
        xg = _dot(lr_ref[...].astype(BF16), wgk_ref[...], NN) + bgk_ref[...]
        for d, (raw_ref, k_ref, g_ref) in enumerate(((hf_ref, kf_ref, gf_ref), (hb_ref, kb_ref, gb_ref))):
            lbd = _hg_lb(lb_ref, d)
            f = lbd + (1.0 - lbd) * _sig(raw_ref[...].astype(F32))
            k_ref[:, :HW] = (1.0 - f).astype(BF16)
            k_ref[:, HW:] = gk_ref[...]
            g_ref[:, :HW] = jnp.log(f)
            g_ref[:, HW:] = _log_sigmoid(xg[:, d * HW:(d + 1) * HW]) * (1.0 / GLA_NORM)

    out = jax.ShapeDtypeStruct((t, D), F32)
    outb = jax.ShapeDtypeStruct((t, D), BF16)
    return pl.pallas_call(
        body, name="gates_fwd", grid=(t // TR,),
        in_specs=[seg(0), seg(1), seg(2), seg(3), seg(5), seg(6), seg(7), _rowcol(DH, LR0 // DH),
                  _full((2, 2, HW)), _full((DH, D)), _full((1, D))],
        out_specs=[_row(D)] * 6,
        out_shape=[outb] * 4 + [out] * 2,
        compiler_params=_cp("parallel"),
    )(p, p, p, p, p, p, p, p, hg_lb, wgk, bgk)


def _post_fwd(o_fw, o_bw, p, onw):
    t = o_fw.shape[0]

    def body(of_ref, ob_ref, g1_ref, g2_ref, w_ref, y_ref):
        for h in range(NH):
            sl = slice(h * DH, (h + 1) * DH)
            o = of_ref[:, sl] + ob_ref[:, sl]
            g_ref = g1_ref if h < NH // 2 else g2_ref
            gs = slice((h % (NH // 2)) * DH, (h % (NH // 2) + 1) * DH)
            n = o * _rstd(o) * w_ref[:, sl]
            y_ref[:, sl] = (n * _silu(g_ref[:, gs].astype(F32))).astype(BF16)

    return pl.pallas_call(
        body, name="post_fwd", grid=(t // TR,),
        in_specs=[_row(D), _row(D), _rowcol(HW, MAIN0 // HW + 4), _rowcol(HW, MAIN0 // HW + 8), _full((1, D))],
        out_specs=_row(D),
        out_shape=jax.ShapeDtypeStruct((t, D), BF16),
        compiler_params=_cp("parallel"),
    )(o_fw, o_bw, p, p, onw)


def _gate_window_specs(col0):
    return [_rowcol(HW, col0 // HW), _rowcol(HW, col0 // HW + 1), _rowcol(DH, (col0 + 2 * HW) // DH)]


def _gate_window(refs):
    return jnp.concatenate([r[...].astype(F32) for r in refs], axis=1)


def _branch_merge(y, w_hg, w_gla, p):
    t = y.shape[0]

    def body(y_ref, wh_ref, wg_ref, a0, a1, a2, b0, b1, b2, u1_ref, u2_ref, m_ref):
        u1 = _dot(y_ref[:, :HW], wh_ref[...], NN)
        u2 = _dot(y_ref[:, HW:], wg_ref[...], NN)
        u1_ref[...] = u1.astype(BF16)
        u2_ref[...] = u2.astype(BF16)
        m_ref[...] = (_sig(_gate_window((a0, a1, a2))) * u1 + _sig(_gate_window((b0, b1, b2))) * u2).astype(BF16)

    out = jax.ShapeDtypeStruct((t, GW), BF16)
    return pl.pallas_call(
        body, name="branch_merge", grid=(t // TR,),
        in_specs=[_row(D), _full((HW, GW)), _full((HW, GW))] + _gate_window_specs(GATE_HG0)
        + _gate_window_specs(GATE_GLA0),
        out_specs=[_row(GW)] * 3, out_shape=[out] * 3,
        compiler_params=_cp("parallel"),
    )(y, w_hg, w_gla, p, p, p, p, p, p)


def _mid_fwd(z, y1, nw_post, nw_pre, modc, modx):
    t = y1.shape[0]

    def body(zc_ref, zx_ref, y_ref, wpo_ref, wpr_ref, mc_ref, mx_ref, z1_ref, h_ref):
        is_ctx = pl.program_id(0) < NCT
        y = y_ref[...].astype(F32)
        z1 = _z_tile(zc_ref, zx_ref, is_ctx) + _mod_row(mc_ref, mx_ref, 2, is_ctx) * (y * _rstd(y) * wpo_ref[...])
        z1_ref[...] = z1
        n = z1 * _rstd(z1) * wpr_ref[...]
        h = n * (1.0 + _mod_row(mc_ref, mx_ref, 4, is_ctx)) + _mod_row(mc_ref, mx_ref, 3, is_ctx)
        h_ref[...] = h.astype(BF16)

    return pl.pallas_call(
        body, name="mid_fwd", grid=(t // TR,),
        in_specs=_z_specs() + [_row(D), _full((1, D)), _full((1, D)), _full((8, D)), _full((8, D))],
        out_specs=[_row(D), _row(D)],
        out_shape=[jax.ShapeDtypeStruct((t, D), F32), jax.ShapeDtypeStruct((t, D), BF16)],
        compiler_params=_cp("parallel"),
    )(*z, y1, nw_post, nw_pre, modc, modx)


def _final(z1, y2, target, nw, modc, modx):
    t = z1.shape[0]

    def body(z1_ref, y_ref, tg_ref, w_ref, mc_ref, mx_ref, dz_ref, dy_ref, loss_ref, sm_ref):
        i = pl.program_id(0)
        is_ctx = i < NCT

        @pl.when(i == 0)
        def _():
            loss_ref[...] = jnp.zeros_like(loss_ref)
            sm_ref[...] = jnp.zeros_like(sm_ref)

        g = _mod_row(mc_ref, mx_ref, 5, is_ctx)
        y = y_ref[...].astype(F32)
        r = _rstd(y)
        w = w_ref[...]
        yr = y * r
        n = yr * w
        e = z1_ref[...] + g * n - tg_ref[...]
        lat = jnp.where(is_ctx, 0.0, 1.0)
        loss_ref[...] += lat * _colsum(e * e)
        dz = e * (lat / D)
        dz_ref[...] = dz
        _acc_mod(sm_ref, 0, is_ctx, _colsum(dz * n))
        dn = dz * g
        _acc_row(sm_ref, 2, _colsum(dn * yr))
        dy_ref[...] = _rms_bwd(dn * w, y, r).astype(BF16)

    return pl.pallas_call(
        body, name="final", grid=(t // TR,),
        in_specs=[_row(D), _row(D), pl.BlockSpec((TR, D), lambda i: (jnp.maximum(i - NCT, 0), 0)),
                  _full((1, D)), _full((8, D)), _full((8, D))],
        out_specs=[_row(D), _row(D), _full((1, D)), _full((8, D))],
        out_shape=[jax.ShapeDtypeStruct((t, D), F32), jax.ShapeDtypeStruct((t, D), BF16),
                   jax.ShapeDtypeStruct((1, D), F32), jax.ShapeDtypeStruct((8, D), F32)],
        compiler_params=_cp("arbitrary"),
    )(z1, y2, target, nw, modc, modx)


def _mid_bwd(dh2, dz, z1, y1, nw_post, nw_pre, modc, modx):
    t = z1.shape[0]

    def body(dh_ref, dz_ref, z1_ref, y_ref, wpo_ref, wpr_ref, mc_ref, mx_ref, dzo_ref, dy_ref, sm_ref):
        i = pl.program_id(0)
        is_ctx = i < NCT

        @pl.when(i == 0)
        def _():
            sm_ref[...] = jnp.zeros_like(sm_ref)

        dh = dh_ref[...].astype(F32)
        z1 = z1_ref[...]
        r = _rstd(z1)
        zr = z1 * r
        wpr = wpr_ref[...]
        n = zr * wpr
        _acc_mod(sm_ref, 0, is_ctx, _colsum(dh))
        _acc_mod(sm_ref, 2, is_ctx, _colsum(dh * n))
        dn = dh * (1.0 + _mod_row(mc_ref, mx_ref, 4, is_ctx))
        _acc_row(sm_ref, 6, _colsum(dn * zr))
        dz1 = dz_ref[...] + _rms_bwd(dn * wpr, z1, r)
        dzo_ref[...] = dz1
        y = y_ref[...].astype(F32)
        r1 = _rstd(y)
        yr = y * r1
        wpo = wpo_ref[...]
        g = _mod_row(mc_ref, mx_ref, 2, is_ctx)
        _acc_mod(sm_ref, 4, is_ctx, _colsum(dz1 * (yr * wpo)))
        dn1 = dz1 * g
        _acc_row(sm_ref, 7, _colsum(dn1 * yr))
        dy_ref[...] = _rms_bwd(dn1 * wpo, y, r1).astype(BF16)

    return pl.pallas_call(
        body, name="mid_bwd", grid=(t // TR,),
        in_specs=[_row(D)] * 4 + [_full((1, D)), _full((1, D)), _full((8, D)), _full((8, D))],
        out_specs=[_row(D), _row(D), _full((8, D))],
        out_shape=[jax.ShapeDtypeStruct((t, D), F32), jax.ShapeDtypeStruct((t, D), BF16),
                   jax.ShapeDtypeStruct((8, D), F32)],
        compiler_params=_cp("arbitrary"),
    )(dh2, dz, z1, y1, nw_post, nw_pre, modc, modx)


def _pre_bwd(dh1, dz, z, nw, modc, modx):
    t = dh1.shape[0]

    def body(dh_ref, dz_ref, zc_ref, zx_ref, w_ref, mc_ref, mx_ref, dzo_ref, sm_ref):
        i = pl.program_id(0)
        is_ctx = i < NCT

        @pl.when(i == 0)
        def _():
            sm_ref[...] = jnp.zeros_like(sm_ref)

        dh = dh_ref[...].astype(F32)
        x = _z_tile(zc_ref, zx_ref, is_ctx)
        r = _rstd(x)
        xr = x * r
        w = w_ref[...]
        _acc_mod(sm_ref, 0, is_ctx, _colsum(dh))
        _acc_mod(sm_ref, 2, is_ctx, _colsum(dh * (xr * w)))
        dn = dh * (1.0 + _mod_row(mc_ref, mx_ref, 1, is_ctx))
        _acc_row(sm_ref, 4, _colsum(dn * xr))
        dzo_ref[...] = dz_ref[...] + _rms_bwd(dn * w, x, r)

    return pl.pallas_call(
        body, name="pre_bwd", grid=(t // TR,),
        in_specs=[_row(D)] * 2 + _z_specs() + [_full((1, D)), _full((8, D)), _full((8, D))],
        out_specs=[pl.BlockSpec((TR, D), lambda i: (jnp.maximum(i - NCT, 0), 0)), _full((8, D))],
        out_shape=[jax.ShapeDtypeStruct((t - CTX, D), F32), jax.ShapeDtypeStruct((8, D), F32)],
        compiler_params=_cp("arbitrary"),
    )(dh1, dz, *z, nw, modc, modx)


def _branch_merge_bwd(dm, p, u1, u2, w_hg, w_gla):
    t = dm.shape[0]

    def body(dm_ref, a0, a1, a2, b0, b1, b2, u1_ref, u2_ref, wh_ref, wg_ref, du1_ref, du2_ref, dg_ref, dyh_ref, dyg_ref):
        dm_ = dm_ref[...].astype(F32)
        s1 = _sig(_gate_window((a0, a1, a2)))
        s2 = _sig(_gate_window((b0, b1, b2)))
        du1 = (dm_ * s1).astype(BF16)
        du2 = (dm_ * s2).astype(BF16)
        du1_ref[...] = du1
        du2_ref[...] = du2
        dg_ref[:, :GW] = (dm_ * u1_ref[...].astype(F32) * s1 * (1.0 - s1)).astype(BF16)
        dg_ref[:, GW:] = (dm_ * u2_ref[...].astype(F32) * s2 * (1.0 - s2)).astype(BF16)
        dyh_ref[...] = _dot(du1, wh_ref[...], NT).astype(BF16)
        dyg_ref[...] = _dot(du2, wg_ref[...], NT).astype(BF16)

    return pl.pallas_call(
        body, name="branch_merge_bwd", grid=(t // TR,),
        in_specs=[_row(GW)] + _gate_window_specs(GATE_HG0) + _gate_window_specs(GATE_GLA0)
        + [_row(GW), _row(GW), _full((HW, GW)), _full((HW, GW))],
        out_specs=[_row(GW), _row(GW), _row(2 * GW), _row(HW), _row(HW)],
        out_shape=[jax.ShapeDtypeStruct((t, GW), BF16), jax.ShapeDtypeStruct((t, GW), BF16),
                   jax.ShapeDtypeStruct((t, 2 * GW), BF16), jax.ShapeDtypeStruct((t, HW), BF16),
                   jax.ShapeDtypeStruct((t, HW), BF16)],
        compiler_params=_cp("parallel"),
    )(dm, p, p, p, p, p, p, u1, u2, w_hg, w_gla)


def _post_bwd(dy_hg, dy_gla, o_fw, o_bw, p, onw):
    t = o_fw.shape[0]

    def body(d1_ref, d2_ref, of_ref, ob_ref, g1_ref, g2_ref, w_ref, do_ref, dg_ref, sm_ref):
        @pl.when(pl.program_id(0) == 0)
        def _():
            sm_ref[...] = jnp.zeros_like(sm_ref)

        for h in range(NH):
            sl = slice(h * DH, (h + 1) * DH)
            gs = slice((h % (NH // 2)) * DH, (h % (NH // 2) + 1) * DH)
            g_ref, d_ref = (g1_ref, d1_ref) if h < NH // 2 else (g2_ref, d2_ref)
            o = of_ref[:, sl] + ob_ref[:, sl]
            r = _rstd(o)
            orr = o * r
            w = w_ref[:, sl]
            gt = g_ref[:, gs].astype(F32)
            dy = d_ref[:, gs].astype(F32)
            dg_ref[:, sl] = (dy * (orr * w) * _dsilu(gt)).astype(BF16)
            dn = dy * _silu(gt)
            sm_ref[0:1, sl] += _colsum(dn * orr)
            do_ref[:, sl] = _rms_bwd(dn * w, o, r)

    return pl.pallas_call(
        body, name="post_bwd", grid=(t // TR,),
        in_specs=[_row(HW), _row(HW), _row(D), _row(D), _rowcol(HW, MAIN0 // HW + 4), _rowcol(HW, MAIN0 // HW + 8),
                  _full((1, D))],
        out_specs=[_row(D), _row(D), _full((8, D))],
        out_shape=[jax.ShapeDtypeStruct((t, D), F32), jax.ShapeDtypeStruct((t, D), BF16),
                   jax.ShapeDtypeStruct((8, D), F32)],
        compiler_params=_cp("arbitrary"),
    )(dy_hg, dy_gla, o_fw, o_bw, p, p, onw)


def _gates_bwd(p, hg_lb, wgk, bgk, dgm, dgo, dq_f, dq_b, dv_f, dv_b, dk_f, dk_b, dg_f, dg_b):
    t = p.shape[0]
    seg = lambda j: _rowcol(HW, MAIN0 // HW + j)

    def body(hq_ref, hf_ref, hb_ref, lr_ref, lb_ref, wgk_ref, bgk_ref, dgm_ref, dgo_ref,
             dqf_ref, dqb_ref, dvf_ref, dvb_ref, dkf_ref, dkb_ref, dgf_ref, dgb_ref,
             dp_ref, dlb_ref, dw_ref, db_ref):
        @pl.when(pl.program_id(0) == 0)
        def _():
            dlb_ref[...] = jnp.zeros_like(dlb_ref)
            dw_ref[...] = jnp.zeros_like(dw_ref)
            db_ref[...] = jnp.zeros_like(db_ref)

        c0 = MAIN0

        def put(j, val):
            dp_ref[:, c0 + j * HW:c0 + (j + 1) * HW] = val.astype(BF16)

        dq = dqf_ref[...].astype(F32) + dqb_ref[...].astype(F32)
        dv = dvf_ref[...].astype(F32) + dvb_ref[...].astype(F32)
        put(0, dq[:, :HW] * _dsilu(hq_ref[...].astype(F32)))
        put(1, dv[:, :HW])
        put(5, dq[:, HW:] * (DH ** -0.5))
        put(7, dv[:, HW:])
        put(6, dkf_ref[:, HW:].astype(F32) + dkb_ref[:, HW:].astype(F32))
        dp_ref[:, c0 + 4 * HW:c0 + 5 * HW] = dgo_ref[:, :HW]
        dp_ref[:, c0 + 8 * HW:c0 + 9 * HW] = dgo_ref[:, HW:]
        lr = lr_ref[...].astype(BF16)
        xg = _dot(lr, wgk_ref[...], NN) + bgk_ref[...]
        dxg = []
        for d, (raw_ref, dk_ref, dg_ref) in enumerate(((hf_ref, dkf_ref, dgf_ref), (hb_ref, dkb_ref, dgb_ref))):
            lbd = _hg_lb(lb_ref, d)
            s = _sig(raw_ref[...].astype(F32))
            f = lbd + (1.0 - lbd) * s
            df = dg_ref[:, :HW] / f - dk_ref[:, :HW].astype(F32)
            put(2 + d, df * (1.0 - lbd) * s * (1.0 - s))
            dlb_ref[d:d + 1, :] += _colsum(df * (1.0 - s)) * (lbd * (1.0 - lbd))
            dxg.append(dg_ref[:, HW:] * (1.0 / GLA_NORM) * _sig(-xg[:, d * HW:(d + 1) * HW]))
        dxg = jnp.concatenate(dxg, axis=1)
        db_ref[0:1, :] += _colsum(dxg)
        dxg_b = dxg.astype(BF16)
        dw_ref[...] += _dot(lr, dxg_b, TN)
        dlr = _dot(dxg_b, wgk_ref[...], NT)
        dp_ref[:, LR0:LR0 + DH] = (dlr + dgm_ref[:, :DH].astype(F32)).astype(BF16)
        dp_ref[:, LR0 + DH:GATE_GLA0] = dgm_ref[:, DH:D]
        dp_ref[:, GATE_GLA0:GATE_GLA0 + DH] = dgm_ref[:, D:GW] + dgm_ref[:, GW:GW + DH]
        dp_ref[:, GATE_GLA0 + DH:GATE_GLA0 + GW] = dgm_ref[:, GW + DH:]
        dp_ref[:, GATE_GLA0 + GW:] = jnp.zeros((TR, W_IN_COLS - GATE_GLA0 - GW), BF16)

    return pl.pallas_call(
        body, name="gates_bwd", grid=(t // TR,),
        in_specs=[seg(0), seg(2), seg(3), _rowcol(DH, LR0 // DH), _full((2, 2, HW)), _full((DH, D)), _full((1, D)),
                  _row(2 * GW), _row(D)] + [_row(D)] * 8,
        out_specs=[_row(W_IN_COLS), _full((8, HW)), _full((DH, D)), _full((8, D))],
        out_shape=[jax.ShapeDtypeStruct((t, W_IN_COLS), BF16), jax.ShapeDtypeStruct((8, HW), F32),
                   jax.ShapeDtypeStruct((DH, D), F32), jax.ShapeDtypeStruct((8, D), F32)],
        compiler_params=_cp("arbitrary"),
    )(p, p, p, p, hg_lb, wgk, bgk, dgm, dgo, dq_f, dq_b, dv_f, dv_b, dk_f, dk_b, dg_f, dg_b)


def _scan_consts(rev):
    r = lax.broadcasted_iota(jnp.int32, (CHUNK, CHUNK), 0)
    u = lax.broadcasted_iota(jnp.int32, (CHUNK, CHUNK), 1)
    rp = lax.broadcasted_iota(jnp.int32, (CHUNK, 1), 0)
    if rev:
        r, u, rp = CHUNK - 1 - r, CHUNK - 1 - u, CHUNK - 1 - rp
    tri = jnp.where(u <= r, 1.0, 0.0).astype(F32)
    tri_t = jnp.where(r <= u, 1.0, 0.0).astype(F32)
    lv = []
    for b in LEVELS:
        sh = b.bit_length() - 1
        pair = ((r >> sh) == (u >> sh) + 1) & (((u >> sh) & 1) == 0)
        pair_t = ((u >> sh) == (r >> sh) + 1) & (((r >> sh) & 1) == 0)
        tside = ((rp >> sh) & 1) == 1
        lv.append((pair, pair_t, tside, jnp.where(tside, 1.0, -1.0).astype(F32)))
    bd = LEVELS[-1].bit_length() - 1
    diag = ((r >> bd) == (u >> bd)) & (u <= r)
    diag_t = ((r >> bd) == (u >> bd)) & (r <= u)
    return tri, tri_t, lv, diag, diag_t


def _row_of(pos, rev):
    return CHUNK - 1 - pos if rev else pos


def _chunk_terms(cum, b_scr, consts, rev):
    _, _, lv, _, _ = consts
    terms = []
    for b, (_, _, _, sgn) in zip(LEVELS, lv):
        pieces = []
        for j in range(CHUNK // (2 * b)):
            row = _row_of(2 * b * j + b - 1, rev)
            pieces.append(jnp.broadcast_to(b_scr[row:row + 1, :], (2 * b, DH)))
        if rev:
            pieces = pieces[::-1]
        bnd = pieces[0] if len(pieces) == 1 else jnp.concatenate(pieces, axis=0)
        terms.append(jnp.exp((cum - bnd) * sgn))
    b = LEVELS[-1]
    pieces = []
    for j in range(CHUNK // b):
        if j == 0:
            pieces.append(jnp.zeros((b, DH), F32))
        else:
            row = _row_of(b * j - 1, rev)
            pieces.append(jnp.broadcast_to(b_scr[row:row + 1, :], (b, DH)))
    if rev:
        pieces = pieces[::-1]
    start = jnp.concatenate(pieces, axis=0)
    wq = jnp.exp(jnp.minimum(cum - start, 0.0))
    wk = jnp.exp(jnp.minimum(start - cum, EXP_CLAMP))
    terms.append((wq, wk))
    return terms


def _run_staged(units):
    live = list(units)
    while live:
        nxt = []
        for u in live:
            try:
                next(u)
                nxt.append(u)
            except StopIteration:
                pass
        live = nxt


SCAN_TB = 256
SCAN_CB = SCAN_TB // CHUNK


def _block_order(i, ntb, rev):
    nctx = CTX // SCAN_TB
    if not rev:
        return i
    return jnp.where(i < nctx, nctx - 1 - i, ntb - 1 - (i - nctx))


def _chunk_in_block(j, rev):
    return SCAN_CB - 1 - j if rev else j


def _scan_fwd(q, k, v, g, rev):
    t = q.shape[0]
    nc = t // CHUNK
    hpb = SCAN_HEADS_FWD

    def body(q_ref, k_ref, v_ref, g_ref, o_ref, st_ref, s_scr, b_scr):
        consts = _scan_consts(rev)
        _, _, lv, diag, _ = consts
        masks = [lvl[0] for lvl in lv] + [diag]

        @pl.when(pl.program_id(1) == 0)
        def _():
            s_scr[...] = jnp.zeros_like(s_scr)

        tri = consts[0]
        state = {hh: s_scr[hh] for hh in range(hpb)}

        def unit(hh, j):
            sl = slice(hh * DH, (hh + 1) * DH)
            c = _chunk_in_block(j, rev)
            rows = slice(c * CHUNK, (c + 1) * CHUNK)
            b_ref = b_scr.at[hh * SCAN_CB + j]
            qc, kc, vc, gc = q_ref[rows, sl], k_ref[rows, sl], v_ref[rows, sl], g_ref[rows, sl]
            cum = _split_dot(tri, gc)
            b_ref[...] = cum
            yield
            terms = _chunk_terms(cum, b_ref, consts, rev)
            qf, kf = qc.astype(F32), kc.astype(F32)
            xs = [(jnp.where(tside, qf, kf) * w).astype(BF16) for w, (_, _, tside, _) in zip(terms[:-1], lv)]
            qd, kd = (qf * terms[-1][0]).astype(BF16), (kf * terms[-1][1]).astype(BF16)
            tot = _colsum(gc)
            qe = (qf * jnp.exp(cum)).astype(BF16)
            ke = (kf * jnp.exp(tot - cum)).astype(BF16)
            vb = vc.astype(BF16)
            yield
            scs = [_dot(x, x, NT) for x in xs] + [_dot(qd, kd, NT)]
            kv = _dot(vb, ke, TN)
            yield
            a = jnp.zeros((CHUNK, CHUNK), F32)
            for sc, m in zip(scs, masks):
                a = a + jnp.where(m, sc, 0.0)
            o_intra = _dot(a.astype(BF16), vb, NN)
            yield
            st = state[hh]
            st_ref[hh, c] = st
            o_ref[rows, sl] = o_intra + _dot(qe, st.astype(BF16), NT)
            state[hh] = st * jnp.exp(tot) + kv
            yield

        _run_staged([unit(hh, j) for hh in range(hpb) for j in range(SCAN_CB)])
        for hh in range(hpb):
            s_scr[hh] = state[hh]

    ntb = t // SCAN_TB
    col = pl.BlockSpec((SCAN_TB, hpb * DH), lambda h, i: (_block_order(i, ntb, rev), h))
    return pl.pallas_call(
        body, name="scan_fwd_" + ("bw" if rev else "fw"), grid=(NH // hpb, ntb),
        in_specs=[col] * 4,
        out_specs=[col, pl.BlockSpec((hpb, SCAN_CB, DH, DH), lambda h, i: (h, _block_order(i, ntb, rev), 0, 0))],
        out_shape=[jax.ShapeDtypeStruct((t, D), F32), jax.ShapeDtypeStruct((NH, nc, DH, DH), F32)],
        scratch_shapes=[pltpu.VMEM((hpb, DH, DH), F32), pltpu.VMEM((hpb * SCAN_CB, CHUNK, DH), F32)],
        compiler_params=_cp("parallel", "arbitrary"),
    )(q, k, v, g)


def _scan_bwd(q, k, v, g, do, states, rev):
    t = q.shape[0]
    nc = t // CHUNK
    hpb = SCAN_HEADS_BWD

    def body(q_ref, k_ref, v_ref, g_ref, do_ref, st_ref, dq_ref, dk_ref, dv_ref, dg_ref, ds_scr, b_scr):
        consts = _scan_consts(rev)
        _, tri_t, lv, diag, diag_t = consts
        masks = [(lvl[0], lvl[1]) for lvl in lv] + [(diag, diag_t)]
        @pl.when(pl.program_id(1) == 0)
        def _():
            ds_scr[...] = jnp.zeros_like(ds_scr)

        tri = consts[0]
        dstate = {hh: ds_scr[hh] for hh in range(hpb)}

        def unit(hh, jj):
            sl = slice(hh * DH, (hh + 1) * DH)
            c = _chunk_in_block(SCAN_CB - 1 - jj, rev)
            rows = slice(c * CHUNK, (c + 1) * CHUNK)
            b_ref = b_scr.at[hh * SCAN_CB + jj]
            qc, kc, vc, gc = q_ref[rows, sl], k_ref[rows, sl], v_ref[rows, sl], g_ref[rows, sl]
            dob = do_ref[rows, sl].astype(BF16)
            vb = vc.astype(BF16)
            cum = _split_dot(tri, gc)
            b_ref[...] = cum
            da = _dot(dob, vb, NT)
            da_t = _dot(vb, dob, NT)
            yield
            terms = _chunk_terms(cum, b_ref, consts, rev)
            qf, kf = qc.astype(F32), kc.astype(F32)
            xs = [(jnp.where(tside, qf, kf) * w).astype(BF16) for w, (_, _, tside, _) in zip(terms[:-1], lv)]
            wqd, wkd = terms[-1]
            qdb, kdb = (qf * wqd).astype(BF16), (kf * wkd).astype(BF16)
            tot = _colsum(gc)
            e_tot = jnp.exp(tot)
            e_b = jnp.exp(cum)
            e_t = jnp.exp(tot - cum)
            qeb = (qf * e_b).astype(BF16)
            keb = (kf * e_t).astype(BF16)
            dsym = [(jnp.where(m, da, 0.0) + jnp.where(m_t, da_t, 0.0)).astype(BF16) for m, m_t in masks[:-1]]
            dad = (jnp.where(diag, da, 0.0).astype(BF16), jnp.where(diag_t, da_t, 0.0).astype(BF16))
            yield
            sym = [_dot(x, x, NT) for x in xs]
            dxs = [_dot(d, x, NN) for d, x in zip(dsym, xs)]
            at_d = _dot(kdb, qdb, NT)
            dqt_d = _dot(dad[0], kdb, NN)
            dkt_d = _dot(dad[1], qdb, NN)
            qd = _dot(dob, qeb, TN)
            yield
            a_t = jnp.where(diag_t, at_d, 0.0)
            dq = dqt_d * wqd
            dk = dkt_d * wkd
            db = dqt_d * qdb.astype(F32) - dkt_d * kdb.astype(F32)
            for s, dx, x, w, (_, m_t, tside, sgn) in zip(sym, dxs, xs, terms[:-1], lv):
                a_t = a_t + jnp.where(m_t, s, 0.0)
                dxw = dx * w
                dq = dq + jnp.where(tside, dxw, 0.0)
                dk = dk + jnp.where(tside, 0.0, dxw)
                db = db + (dx * x.astype(F32)) * sgn
            dv_intra = _dot(a_t.astype(BF16), dob, NN)
            st = st_ref[hh, c]
            stb = st.astype(BF16)
            dqe = _dot(dob, stb, NN)
            yield
            dst = dstate[hh]
            dstb = dst.astype(BF16)
            dstate[hh] = dst * e_tot + qd
            dv_ref[rows, sl] = (dv_intra + _dot(keb, dstb, NT)).astype(BF16)
            dke = _dot(vb, dstb, NN)
            yield
            qe = qeb.astype(F32)
            ke = keb.astype(F32)
            dq_ref[rows, sl] = (dq + dqe * e_b).astype(BF16)
            dk_ref[rows, sl] = (dk + dke * e_t).astype(BF16)
            db = db + dqe * qe - dke * ke
            dtot = _colsum(dstb.astype(F32) * stb.astype(F32)) * e_tot + _colsum(dke * ke)
            dg_ref[rows, sl] = _split_dot(tri_t, db) + dtot
            yield

        _run_staged([unit(hh, jj) for hh in range(hpb) for jj in range(SCAN_CB)])
        for hh in range(hpb):
            ds_scr[hh] = dstate[hh]

    ntb = t // SCAN_TB
    blk = lambda i: _block_order(ntb - 1 - i, ntb, rev)
    col = pl.BlockSpec((SCAN_TB, hpb * DH), lambda h, i: (blk(i), h))
    out = jax.ShapeDtypeStruct((t, D), F32)
    outb = jax.ShapeDtypeStruct((t, D), BF16)
    return pl.pallas_call(
        body, name="scan_bwd_" + ("bw" if rev else "fw"), grid=(NH // hpb, ntb),
        in_specs=[col] * 5 + [pl.BlockSpec((hpb, SCAN_CB, DH, DH), lambda h, i: (h, blk(i), 0, 0))],
        out_specs=[col] * 4,
        out_shape=[outb] * 3 + [out],
        scratch_shapes=[pltpu.VMEM((hpb, DH, DH), F32), pltpu.VMEM((hpb * SCAN_CB, CHUNK, DH), F32)],
        compiler_params=_cp("parallel", "arbitrary"),
    )(q, k, v, g, do, states)


W_IN_GRAD_CHUNKS = (("a", (0, 512)), ("b", (0, 256)), ("b", (256, 512)))
W_IN_REF = 6688
W_IN_PAD = 896
W_IN_PIECE = 256
W_IN_STAGES = (3, 4)


def _assemble_w_in(g, rows, prev, name):
    n, r, wp = g.shape
    tr = W_IN_PIECE
    tiles = wp // DH
    first = rows[0] // tr

    def body(g_ref, *refs):
        o_ref = refs[-1]
        lane = lax.broadcasted_iota(jnp.int32, (tr, DH), 1)
        for t in range(W_IN_COLS // DH):
            acc = None
            for j in range(n):
                c = DH * t - W_IN_SHARD * j
                if c <= -DH or c >= W_IN_SHARD:
                    continue
                k, s = divmod(c, DH)
                lo = g_ref[j, :, k * DH:(k + 1) * DH] if 0 <= k < tiles else None
                hi = g_ref[j, :, (k + 1) * DH:(k + 2) * DH] if s and 0 <= k + 1 < tiles else None
                if s:
                    zero = jnp.zeros((tr, DH), g.dtype)
                    lo = zero if lo is None else pltpu.roll(lo, DH - s, 1)
                    hi = zero if hi is None else pltpu.roll(hi, DH - s, 1)
                    part = jnp.where(lane < DH - s, lo, hi)
                else:
                    part = lo
                acc = part if acc is None else acc + part
            o_ref[:, t * DH:(t + 1) * DH] = jnp.zeros((tr, DH), g.dtype) if acc is None else acc

    held = [] if prev is None else [prev]
    return pl.pallas_call(
        body, name=name, grid=((rows[1] - rows[0]) // tr,),
        in_specs=[pl.BlockSpec((n, tr, wp), lambda i: (0, first + i, 0))] + [pl.BlockSpec(memory_space=pl.ANY)] * len(held),
        out_specs=pl.BlockSpec((tr, W_IN_COLS), lambda i: (first + i, 0)),
        out_shape=jax.ShapeDtypeStruct((r, W_IN_COLS), g.dtype),
        input_output_aliases={1: 0} if held else {},
        compiler_params=_cp("parallel"),
    )(g, *held)


def _gate_cols(w):
    return jnp.pad(w, ((0, 0), (GOFF, GW - GOFF - D)))


def _gate_rows(w):
    return jnp.pad(w, ((GOFF, GW - GOFF - D), (0, 0)))


def _layout_wgk(w):
    r = w.shape[1]
    top = jnp.concatenate([w[0], jnp.zeros_like(w[0])], axis=1)
    bot = jnp.concatenate([jnp.zeros_like(w[1]), w[1]], axis=1)
    return jnp.concatenate([top, bot, jnp.zeros((DH - 2 * r, D), w.dtype)], axis=0)


def _unlayout_wgk(d, r=16):
    return jnp.stack([d[:r, :HW], d[r:2 * r, HW:]])


def _local_step(z, target, modc, modx, norms, onw, hg_lb, wgk, bgk, get_w_in, get_mix, get_ffn, send):
    n_pre1, n_post1, n_pre2, n_post2 = norms
    t = z[0].shape[0] + z[1].shape[0]
    tm = 1152 if t % 1152 == 0 else 256
    h1 = _prenorm(z, n_pre1, modc, modx, 0, 1, "prenorm1")
    w_in = get_w_in(h1)
    p = _matmul(h1, w_in, NN, BF16, "mm_in", t, 1024, D)
    q, v, k_f, k_b, g_f, g_b = _gates_fwd(p, hg_lb, wgk, bgk)
    o_f, st_f = _scan_fwd(q, k_f, v, g_f, False)
    o_b, st_b = _scan_fwd(q, k_b, v, g_b, True)
    y = _post_fwd(o_f, o_b, p, onw)
    w_br_hg, w_br_gla, w_out = get_mix(y)
    u1, u2, merged = _branch_merge(y, w_br_hg, w_br_gla, p)
    y1 = _matmul(merged, w_out, NN, BF16, "mm_out", tm, 512, GW)
    z1, h2 = _mid_fwd(z, y1, n_post1, n_pre2, modc, modx)
    w_gu_t, w_down = get_ffn(h2)
    u, v_ff, act = _mm_gu_act(h2, w_gu_t[0], w_gu_t[1], "mm_gu", tm)
    y2 = _matmul(act, w_down, NN, BF16, "mm_down", t, 512, D_FF)
    dz, dy2, loss_vec, sm_final = _final(z1, y2, target, n_post2, modc, modx)
    du, dv_ff = _mm_down_dx_act(dy2, w_down, u, v_ff, "mm_down_dx", tm)
    d_w_down = _matmul(act, dy2, TN, BF16, "mm_down_dw", D_FF // 2, 1024, t)
    dh2 = _matmul((du, dv_ff), w_gu_t, NN, BF16, "mm_gu_dx", tm, 512, D_FF)
    d_w_gate_t = _matmul(du, h2, TN, BF16, "mm_gate_dw", D_FF // 2, 1024, t)
    d_w_up_t = _matmul(dv_ff, h2, TN, BF16, "mm_up_dw", D_FF // 2, 1024, t)
    dh2 = send(("w_down", "w_gate_t", "w_up_t"), (d_w_down, d_w_gate_t, d_w_up_t), dh2)
    dz, dy1, sm_mid = _mid_bwd(dh2, dz, z1, y1, n_post1, n_pre2, modc, modx)
    dmerged = _matmul(dy1, w_out, NT, BF16, "mm_out_dx", tm, GW, D)
    d_w_out = _matmul(merged, dy1, TN, BF16, "mm_out_dw", GW, 512, t)
    du1, du2, dgm, dy_hg, dy_gla = _branch_merge_bwd(dmerged, p, u1, u2, w_br_hg, w_br_gla)
    d_w_br_hg = _matmul(y, du1, TN, BF16, "mm_br_hg_dw", HW, GW, t, a_off=0, m_out=HW)
    d_w_br_gla = _matmul(y, du2, TN, BF16, "mm_br_gla_dw", HW, GW, t, a_off=1, m_out=HW)
    dy_hg = send(("w_out", "w_br_hg", "w_br_gla"), (d_w_out, d_w_br_hg, d_w_br_gla), dy_hg)
    do, dgo, sm_post = _post_bwd(dy_hg, dy_gla, o_f, o_b, p, onw)
    dq_f, dk_f, dv_f, dg_f = _scan_bwd(q, k_f, v, g_f, do, st_f, False)
    dq_b, dk_b, dv_b, dg_b = _scan_bwd(q, k_b, v, g_b, do, st_b, True)
    dp, d_lb, d_wgk, d_bgk = _gates_bwd(p, hg_lb, wgk, bgk, dgm, dgo, dq_f, dq_b, dv_f, dv_b, dk_f, dk_b, dg_f, dg_b)
    d_w_in_a = _matmul(h1, dp, TN, BF16, "mm_in_dw_a", 512, 1024, t, a_off=0, m_out=D // 2)
    dp = send(("w_in_a",), (d_w_in_a,), dp)
    d_w_in_b = _matmul(h1, dp, TN, BF16, "mm_in_dw_b", 512, 1024, t, a_off=1, m_out=D // 2)
    dp = send(("w_in_b",), (d_w_in_b,), dp)
    dh1 = _matmul(dp, w_in, NT, BF16, "mm_in_dx", tm, 512, W_IN_COLS // 2)
    grad_x, sm_pre = _pre_bwd(dh1, dz, z, n_pre1, modc, modx)
    return dict(loss_vec=loss_vec, grad_x=grad_x, sm_final=sm_final, sm_mid=sm_mid, sm_post=sm_post, sm_pre=sm_pre,
                d_lb=d_lb, d_wgk=d_wgk, d_bgk=d_bgk)


MESH = pl.DeviceIdType.MESH
ANY = pl.BlockSpec(memory_space=pl.ANY)
N_REL = N_DEV - 1


def _place():
    return lax.axis_index("x"), lax.axis_index("y"), lax.axis_index("c")


def _slot(p):
    return 4 * p[0] + 2 * p[1] + p[2]


HBM = pl.BlockSpec(memory_space=pltpu.HBM)
SEM = pl.BlockSpec(memory_space=pltpu.SEMAPHORE)
EFFECT = pltpu.SideEffectType.DATAFLOW_SIDE_EFFECTING


def _peer_of(x, y, c, k):
    flip = lambda v, bit: 1 - v if bit else v
    return flip(x, k & 4), flip(y, k & 2), flip(c, k & 1)


def _view_whole(src, slot):
    return src


def _view_near(src, slot):
    return src


_view_near.peers = (1, 2, 4, 6)


def _view_near_rows(rows):
    def view(src, slot):
        return src.at[pl.ds(rows[0], rows[1] - rows[0])]
    view.peers = _view_near.peers
    view.land = lambda land, slot: land.at[slot, pl.ds(rows[0], rows[1] - rows[0])]
    return view


def _view_block(src, slot):
    return src.at[slot]


def _view_cols(src, slot):
    return src.at[:, pl.ds(pl.multiple_of(slot * (D // N_DEV), D // N_DEV), D // N_DEV)]


W_IN_SHARD = W_IN_REF // N_DEV


def _view_window(rows):
    def view(src, slot):
        col0 = pl.multiple_of((W_IN_SHARD * slot // DH) * DH, DH)
        return src.at[pl.ds(rows[0], rows[1] - rows[0]), pl.ds(col0, D)]
    return view


def _split_copies(view, srcs, lands, send_sems, recv_sems, local_sems):
    x, y, c = _place()
    me = _slot((x, y, c))
    into = getattr(view, "land", lambda land, slot: land.at[slot])
    local, sends, waits = [], [], []
    for a, (src, land) in enumerate(zip(srcs, lands)):
        local.append(pltpu.make_async_copy(view(src, me), into(land, me), local_sems.at[a]))
        for k in getattr(view, "peers", range(1, N_DEV)):
            peer = _peer_of(x, y, c, k)
            mine = view(src, _slot(peer))
            sems = dict(send_sem=send_sems.at[N_REL * a + k - 1], recv_sem=recv_sems.at[N_REL * a + k - 1],
                        device_id=peer, device_id_type=MESH)
            sends.append(pltpu.make_async_remote_copy(src_ref=mine, dst_ref=into(land, me), **sems))
            waits.append(pltpu.make_async_remote_copy(src_ref=mine, dst_ref=into(land, _slot(peer)), **sems))
    return local, sends, waits


def _split_start(groups, name, after):
    built = []
    for view, srcs, lands in groups:
        lands = [lax.empty(l, s.dtype) if isinstance(l, tuple) else l for l, s in zip(lands, srcs)]
        built.append((view, list(srcs), lands))
    bufs = [b for _, srcs, lands in built for b in srcs + lands]
    nb, ng = len(bufs), len(built)

    def body(*refs):
        buf_refs, sem_refs, token = refs[:nb], refs[nb + 1:nb + 1 + 3 * ng], refs[-1]
        pos = 0
        for i, (view, srcs, _) in enumerate(built):
            n = len(srcs)
            local, sends, _ = _split_copies(view, buf_refs[pos:pos + n], buf_refs[pos + n:pos + 2 * n],
                                            *sem_refs[3 * i:3 * i + 3])
            pos += 2 * n
            for cp in local + sends:
                cp.start()
        token[...] = jnp.zeros_like(token)

    sems = []
    for _, srcs, _ in built:
        n = len(srcs)
        sems += [pltpu.SemaphoreType.DMA((N_REL * n,)), pltpu.SemaphoreType.DMA((N_REL * n,)),
                 pltpu.SemaphoreType.DMA((n,))]
    hbm = lambda a: pltpu.with_memory_space_constraint(a, pltpu.HBM)
    out = pl.pallas_call(
        body, name=name,
        out_shape=(*sems, *[pltpu.HBM(b.shape, b.dtype) for b in bufs], jax.ShapeDtypeStruct((8, DH), F32)),
        in_specs=[HBM] * nb + [ANY],
        out_specs=(*([SEM] * (3 * ng)), *([HBM] * nb), pl.BlockSpec(memory_space=pltpu.VMEM)),
        input_output_aliases={i: 3 * ng + i for i in range(nb)},
        compiler_params=pltpu.CompilerParams(has_side_effects=EFFECT),
    )(*[hbm(b) for b in bufs], after)
    handles, pos = [], 3 * ng
    for i, (view, srcs, _) in enumerate(built):
        n = len(srcs)
        handles.append(dict(view=view, n=n, sems=out[3 * i:3 * i + 3], srcs=list(out[pos:pos + n]),
                            lands=list(out[pos + n:pos + 2 * n])))
        pos += 2 * n
    return handles, out[-1]


def _split_wait(handle, name, after, srcs=None, lands=None):
    view, n, sems = handle["view"], handle["n"], handle["sems"]
    srcs = handle["srcs"] if srcs is None else srcs
    lands = handle["lands"] if lands is None else lands
    afters = list(after) if isinstance(after, (list, tuple)) else [after]

    def body(*refs):
        src_refs, land_refs = refs[:n], refs[n:2 * n]
        send_sems, recv_sems, local_sems = refs[2 * n:2 * n + 3]
        local, _, waits = _split_copies(view, src_refs, land_refs, send_sems, recv_sems, local_sems)
        for cp in waits:
            cp.wait_send()
            cp.wait_recv()
        for cp in local:
            cp.wait()

    out = pl.pallas_call(
        body, name=name,
        out_shape=(*[pltpu.HBM(s.shape, s.dtype) for s in srcs], *[pltpu.HBM(l.shape, l.dtype) for l in lands]),
        in_specs=[HBM] * (2 * n) + [SEM, SEM, SEM] + [ANY] * len(afters),
        out_specs=tuple([HBM] * (2 * n)),
        input_output_aliases={i: i for i in range(2 * n)},
        compiler_params=pltpu.CompilerParams(has_side_effects=EFFECT),
    )(*srcs, *lands, *sems, *afters)
    handle["srcs"] = list(out[:n])
    return list(out[n:])


def _tie(x, token, name):
    def body(x_ref, t_ref, o_ref):
        pass

    return pl.pallas_call(
        body, name=name, out_shape=jax.ShapeDtypeStruct(x.shape, x.dtype),
        in_specs=[ANY, ANY], out_specs=ANY, input_output_aliases={0: 0},
    )(x, token)


def _forward_to_sibling(land, name, rows):
    def body(land_ref, out_ref, send_sems, recv_sems):
        x, y, c = _place()
        sibling = (x, y, 1 - c)
        chips = [(1 - x, y), (x, 1 - y), (1 - x, 1 - y)]
        piece = pl.ds(rows[0], rows[1] - rows[0])

        def copy(j, core):
            blk = _slot((*chips[j], core))
            return pltpu.make_async_remote_copy(src_ref=land_ref.at[blk, piece], dst_ref=out_ref.at[blk, piece],
                                                send_sem=send_sems.at[j], recv_sem=recv_sems.at[j],
                                                device_id=sibling, device_id_type=MESH)

        sends = [copy(j, c) for j in range(3)]
        for cp in sends:
            cp.start()
        for j in range(3):
            copy(j, 1 - c).wait_recv()
        for cp in sends:
            cp.wait_send()

    return pl.pallas_call(
        body, name=name, in_specs=[ANY], out_specs=ANY, input_output_aliases={0: 0},
        out_shape=jax.ShapeDtypeStruct(land.shape, land.dtype),
        scratch_shapes=[pltpu.SemaphoreType.DMA((3,)), pltpu.SemaphoreType.DMA((3,))],
    )(land)


def _mod_fwd(a, w, b):
    def body(a_ref, w_ref, b_ref, o_ref):
        o_ref[...] = _dot(_silu(a_ref[...]), w_ref[...], NN, precision=HI) + b_ref[...]

    return pl.pallas_call(
        body, name="mod_fwd", out_shape=jax.ShapeDtypeStruct((a.shape[0], w.shape[1]), F32),
        compiler_params=pltpu.CompilerParams(vmem_limit_bytes=VMEM_LIMIT),
    )(a, w, b)


def _mod_bwd(a, d, w):
    def body(a_ref, d_ref, w_ref, dw_ref, dc_ref):
        av = a_ref[...]
        dv = d_ref[...]
        dw_ref[...] = _dot(_silu(av), dv, TN, precision=HI)
        da = _dot(dv[0:8, :], w_ref[...], NT, precision=HI) * _dsilu(av[0:8, :])
        row = lax.broadcasted_iota(jnp.int32, da.shape, 0)
        dc_ref[...] = jnp.where(row == 0, da, 0.0)

    return pl.pallas_call(
        body, name="mod_bwd",
        out_shape=[jax.ShapeDtypeStruct(w.shape, F32), jax.ShapeDtypeStruct((8, w.shape[0]), F32)],
        compiler_params=pltpu.CompilerParams(vmem_limit_bytes=VMEM_LIMIT),
    )(a, d, w)


def _sum_devices(g):
    def body(g_ref, o_ref):
        acc = g_ref[0]
        for i in range(1, g.shape[0]):
            acc = acc + g_ref[i]
        o_ref[...] = acc

    return pl.pallas_call(body, name="sum_devices_%d" % g.shape[1],
                          out_shape=jax.ShapeDtypeStruct(g.shape[1:], F32))(g)


def _sum_windows(g, name):
    n, r, c = g.shape
    tr = 128

    def body(g_ref, o_ref):
        x, y, cc = _place()
        lane0 = (W_IN_SHARD * _slot((x, y, cc))) % DH
        acc = g_ref[0].astype(F32)
        for i in range(1, n):
            acc = acc + g_ref[i].astype(F32)
        o_ref[...] = pltpu.roll(acc, (c - lane0) % c, 1).T

    return pl.pallas_call(
        body, name=name, grid=(r // tr,),
        in_specs=[pl.BlockSpec((n, tr, c), lambda i: (0, i, 0))],
        out_specs=pl.BlockSpec((c, tr), lambda i: (0, i)),
        out_shape=jax.ShapeDtypeStruct((c, r), F32),
        compiler_params=_cp("parallel"),
    )(g)


def _adam_rows(r, c, n):
    budget = 10 * 1024 * 1024
    best = None
    for tr in range(16, r + 1, 16):
        if r % tr == 0 and tr * c * (2 * n + 28) <= budget:
            best = tr
    return best if best is not None else r


def _adamw(g, w, m, v, name):
    n, r, c = g.shape
    tr = _adam_rows(r, c, n)
    bc1 = 1.0 - ADAM_B1 ** ADAM_STEP
    bc2 = 1.0 - ADAM_B2 ** ADAM_STEP

    def body(g_ref, w_ref, m_ref, v_ref, go_ref, d_ref, mo_ref, vo_ref):
        grad = g_ref[0].astype(F32)
        for i in range(1, n):
            grad = grad + g_ref[i].astype(F32)
        go_ref[...] = grad
        m_new = ADAM_B1 * m_ref[...] + (1.0 - ADAM_B1) * grad
        v_new = ADAM_B2 * v_ref[...] + (1.0 - ADAM_B2) * (grad * grad)
        mo_ref[...] = m_new
        vo_ref[...] = v_new
        d_ref[...] = -ADAM_LR * ((m_new / bc1) / (jnp.sqrt(v_new / bc2) + ADAM_EPS) + ADAM_WD * w_ref[...])

    blk = pl.BlockSpec((tr, c), lambda i: (i, 0))
    out = jax.ShapeDtypeStruct((r, c), F32)
    return pl.pallas_call(
        body, name=name, grid=(r // tr,),
        in_specs=[pl.BlockSpec((n, tr, c), lambda i: (0, i, 0)), blk, blk, blk],
        out_specs=[blk] * 4, out_shape=[out] * 4,
        compiler_params=_cp("parallel"),
    )(g, w, m, v)


ADAM_ROWS3 = 168


def _adam_math(grad, w, m, v):
    bc1 = 1.0 - ADAM_B1 ** ADAM_STEP
    bc2 = 1.0 - ADAM_B2 ** ADAM_STEP
    m_new = ADAM_B1 * m + (1.0 - ADAM_B1) * grad
    v_new = ADAM_B2 * v + (1.0 - ADAM_B2) * (grad * grad)
    delta = -ADAM_LR * ((m_new / bc1) / (jnp.sqrt(v_new / bc2) + ADAM_EPS) + ADAM_WD * w)
    return delta, m_new, v_new


def _adamw_rows3(g, w3, m3, v3, name, cols, prev):
    r, _, _ = w3.shape
    c = cols[1] - cols[0]
    n = min(-(-r // 16) * 8, ADAM_ROWS3 * D // c // 8 * 8)
    starts = list(range(0, r - n, n)) + [r - n]
    held = [] if prev is None else list(prev)

    def body(g_hbm, w_hbm, m_hbm, v_hbm, *refs):
        go_hbm, d_hbm, mo_hbm, vo_hbm, gbuf, ibuf, obuf, in_sems, out_sems = refs[len(held):]
        part = lambda h, r0: h.at[pl.ds(r0, n), 0, pl.ds(cols[0], c)]

        def fetch(p):
            r0, slot = starts[p], p % 2
            g0 = (r0 // 8) * 8
            cps = [pltpu.make_async_copy(g_hbm.at[pl.ds(g0, n + 8)], gbuf.at[slot], in_sems.at[slot, 0])]
            cps += [pltpu.make_async_copy(part(h, r0), ibuf.at[slot, k], in_sems.at[slot, 1 + k])
                    for k, h in enumerate((w_hbm, m_hbm, v_hbm))]
            for cp in cps:
                cp.start()
            return cps

        pending, outs = fetch(0), []
        for p, r0 in enumerate(starts):
            slot = p % 2
            nxt = fetch(p + 1) if p + 1 < len(starts) else []
            for cp in pending:
                cp.wait()
            grad = gbuf[slot, pl.ds(r0 - (r0 // 8) * 8, n), :]
            delta, m_new, v_new = _adam_math(grad, ibuf[slot, 0], ibuf[slot, 1], ibuf[slot, 2])
            for cp in outs:
                cp.wait()
            for k, val in enumerate((grad, delta, m_new, v_new)):
                obuf[slot, k] = val
            outs = [pltpu.make_async_copy(obuf.at[slot, k], part(h, r0), out_sems.at[slot, k])
                    for k, h in enumerate((go_hbm, d_hbm, mo_hbm, vo_hbm))]
            for cp in outs:
                cp.start()
            pending = nxt
        for cp in outs:
            cp.wait()

    out = jax.ShapeDtypeStruct(w3.shape, F32)
    return pl.pallas_call(
        body, name=name, in_specs=[ANY] * (4 + len(held)), out_specs=[ANY] * 4, out_shape=[out] * 4,
        input_output_aliases={4 + k: k for k in range(len(held))},
        scratch_shapes=[pltpu.VMEM((2, n + 8, c), F32), pltpu.VMEM((2, 3, n, c), F32), pltpu.VMEM((2, 4, n, c), F32),
                        pltpu.SemaphoreType.DMA((2, 4)), pltpu.SemaphoreType.DMA((2, 4))],
        compiler_params=pltpu.CompilerParams(vmem_limit_bytes=VMEM_LIMIT),
    )(g, w3, m3, v3, *held)


def kernel(x, c, ctx, c_ctx, w_mod, b_mod, norm_pre1, norm_post1, norm_pre2, norm_post2, w_in, hg_lb, hg_onorm, gla_w_gk, gla_b_gk, gla_onorm, w_br_hg, w_br_gla, w_out, w_ff_gate, w_ff_up, w_ff_down, loss_target, m_c_ctx, m_w_mod, m_b_mod, m_norm_pre1, m_norm_post1, m_norm_pre2, m_norm_post2, m_w_in, m_hg_lb, m_hg_onorm, m_gla_w_gk, m_gla_b_gk, m_gla_onorm, m_w_br_hg, m_w_br_gla, m_w_out, m_w_ff_gate, m_w_ff_up, m_w_ff_down, v_c_ctx, v_w_mod, v_b_mod, v_norm_pre1, v_norm_post1, v_norm_pre2, v_norm_post2, v_w_in, v_hg_lb, v_hg_onorm, v_gla_w_gk, v_gla_b_gk, v_gla_onorm, v_w_br_hg, v_w_br_gla, v_w_out, v_w_ff_gate, v_w_ff_up, v_w_ff_down):
    xi, yi, ci = lax.axis_index("x"), lax.axis_index("y"), lax.axis_index("c")
    me = 4 * xi + 2 * yi + ci
    t = CTX + x.shape[1]

    w_in_pieces, w_in_state = [], {}

    def w_in_piece(i):
        return (_view_near_rows((i * W_IN_PIECE, (i + 1) * W_IN_PIECE)), w_in_state["src"], w_in_state["land"])

    def started_w_in(handle):
        w_in_state.update(src=handle["srcs"], land=handle["lands"])
        w_in_pieces.append(handle)

    tr_ = lambda a: jnp.swapaxes(a[0], 0, 1)
    w_in_bf = jnp.pad(w_in[0].astype(BF16), ((0, 0), (0, W_IN_PAD - W_IN_SHARD)))
    w_in_state.update(src=[w_in_bf], land=[lax.empty((N_DEV,) + w_in_bf.shape, BF16)])
    gathered = lambda arrs: [(N_DEV,) + a.shape for a in arrs]
    whole = lambda arrs: (_view_whole, arrs, gathered(arrs))
    small_in = [c, hg_lb, gla_w_gk[0], gla_b_gk[0]]
    (small_handle, piece), tok = _split_start([whole(small_in), w_in_piece(0)], "ag_small_start", c)
    started_w_in(piece)
    c_all, lb_g, wgk_g, bgk_g = _split_wait(small_handle, "ag_small_wait", tok)
    big = [w_in[0], w_br_hg[0], w_br_gla[0], w_out[0], tr_(w_ff_gate), tr_(w_ff_up), w_ff_down[0]]
    big_bf = [None] + [w.astype(BF16) for w in big[1:]]
    cols = lambda g: jnp.transpose(g, (1, 0, 2)).reshape(g.shape[1], N_DEV * g.shape[2])

    def get_w_in(after):
        w_full, first = None, 0
        for s, last in enumerate(W_IN_STAGES):
            for i in range(first, last):
                land = _split_wait(w_in_pieces[i], "ag_w_in_wait%d" % i, after if w_full is None else [after, w_full],
                                   srcs=w_in_state["src"], lands=w_in_state["land"])
                w_in_state.update(src=w_in_pieces[i]["srcs"], land=land)
            rows = (first * W_IN_PIECE, last * W_IN_PIECE)
            w_in_state["land"] = [_forward_to_sibling(w_in_state["land"][0], "ag_w_in_forward%d" % s, rows)]
            w_full = _assemble_w_in(w_in_state["land"][0], rows, w_full, "assemble_w_in%d" % s)
            first = last
        return w_full

    def get_mix(after):
        g_brh, g_brg, g_out = _split_wait(mix_handle, "ag_mix_wait", after)
        return _gate_cols(cols(g_brh)), _gate_cols(cols(g_brg)), _gate_rows(g_out.reshape(D, D))

    def get_ffn(after):
        g_gate, g_up, g_down = _split_wait(ffn_handle, "ag_ffn_wait", after)
        return (g_gate.reshape(D_FF, D), g_up.reshape(D_FF, D)), g_down.reshape(D_FF, D)

    hg_lb_full = jnp.transpose(lb_g, (1, 2, 0, 3)).reshape(2, 2, HW)
    wgk_k = _layout_wgk(jnp.transpose(wgk_g, (1, 2, 0, 3)).reshape(2, 16, HW)).astype(BF16)
    bgk_k = jnp.transpose(bgk_g, (1, 0, 2)).reshape(1, D)
    onw = jnp.concatenate([jnp.tile(hg_onorm, (1, NH // 2)), jnp.tile(gla_onorm, (1, NH // 2))], axis=1)

    n_mod = w_mod.shape[2]
    a9 = jnp.concatenate([c_ctx[None], c_all[:, 0], jnp.zeros((16 - 1 - N_DEV, D), F32)], axis=0)
    b_loc = lax.dynamic_slice(b_mod, (0, me * n_mod), (1, n_mod))
    s_loc = _mod_fwd(a9, w_mod[0], b_loc)
    (mod_handle, piece), tok = _split_start([whole([s_loc]), w_in_piece(1)], "ag_mod_start", s_loc)
    started_w_in(piece)
    for i in range(2, D // W_IN_PIECE):
        (piece,), tok = _split_start([w_in_piece(i)], "ag_w_in_start%d" % i, tok)
        started_w_in(piece)
    s_all, = _split_wait(mod_handle, "ag_mod_wait", tok)
    mod_all = jnp.transpose(s_all, (1, 0, 2)).reshape(16, N_DEV * n_mod)
    pad8 = lambda m: jnp.concatenate([m.reshape(6, D), jnp.zeros((2, D), F32)], axis=0)
    modc = pad8(mod_all[0])
    modx = pad8(lax.dynamic_slice(mod_all, (1 + me, 0), (1, N_DEV * n_mod))[0])

    (mix_handle, ffn_handle), tok = _split_start([whole(big_bf[1:4]), whole(big_bf[4:])], "ag_big_start", s_all)

    z = (ctx[0], x[0])
    modx = _tie(modx, tok, "tie_mod")
    norms = (norm_pre1, norm_post1, norm_pre2, norm_post2)
    rowshard = lambda d: d.reshape(N_DEV, d.shape[0] // N_DEV, d.shape[1]).astype(BF16)
    sent, w_in_grad = [], {}

    def w_in_chunk(i):
        half, rows = W_IN_GRAD_CHUNKS[i]
        return (_view_window(rows), w_in_grad[half], [(N_DEV, rows[1] - rows[0], D)])

    def sent_w_in(i, handle):
        w_in_grad[W_IN_GRAD_CHUNKS[i][0]] = handle["srcs"]
        sent.append(("w_in%d" % i, ["w_in#%d" % i], handle))

    def send(names, grads, x_after):
        if names == ("w_in_a",):
            w_in_grad["a"] = list(grads)
            (handle,), tok = _split_start([w_in_chunk(0)], "grads_w_in0_start", x_after)
            sent_w_in(0, handle)
            return _tie(x_after, tok, "tie_w_in0")
        if names == ("w_in_b",):
            w_in_grad["b"] = list(grads)
            return x_after
        arrs, leaves, col_arrs, col_leaves = [], [], [], []
        for nm, g in zip(names, grads):
            if nm in ("w_gate_t", "w_up_t"):
                arrs.append(rowshard(g))
                leaves.append({"w_gate_t": "w_ff_gate", "w_up_t": "w_ff_up"}[nm])
            elif nm == "w_down":
                arrs.append(rowshard(g))
                leaves.append("w_ff_down")
            elif nm == "w_out":
                arrs.append(rowshard(g[GOFF:GOFF + D]))
                leaves.append(nm)
            else:
                col_arrs.append(g[:, GOFF:GOFF + D])
                col_leaves.append(nm)
        groups = [(_view_block, arrs, [a.shape for a in arrs])]
        if col_arrs:
            groups.append((_view_cols, col_arrs, [(N_DEV, a.shape[0], D // N_DEV) for a in col_arrs]))
        handles, tok = _split_start(groups, "grads_%s_start" % names[0], x_after)
        sent.append((names[0], leaves, handles[0]))
        if col_arrs:
            sent.append((names[0] + "_cols", col_leaves, handles[1]))
        return _tie(x_after, tok, "tie_" + names[0])

    r = _local_step(z, loss_target[0], modc, modx, norms, onw, hg_lb_full, wgk_k, bgk_k,
                    get_w_in, get_mix, get_ffn, send)
    grad_x = r["grad_x"][None]

    sm_pre, sm_mid, sm_fin = r["sm_pre"], r["sm_mid"], r["sm_final"]
    dmodc = jnp.stack([sm_pre[0], sm_pre[2], sm_mid[4], sm_mid[0], sm_mid[2], sm_fin[0]]).reshape(-1)
    dmodx = jnp.stack([sm_pre[1], sm_pre[3], sm_mid[5], sm_mid[1], sm_mid[3], sm_fin[1]]).reshape(-1)
    on = r["sm_post"][0].reshape(NH, DH)
    pieces = [dmodc, dmodx, sm_pre[4], sm_mid[7], sm_mid[6], sm_fin[2], on[:NH // 2].sum(0), on[NH // 2:].sum(0),
              r["d_lb"][:2].reshape(-1), _unlayout_wgk(r["d_wgk"]).reshape(-1), r["d_bgk"][0]]
    loss_local = (0.5 / D) * jnp.sum(r["loss_vec"])
    pieces.append(jnp.concatenate([loss_local.reshape(1), jnp.zeros((DH - 1,), F32)]))
    sizes = [p.shape[0] for p in pieces]
    pack = jnp.concatenate(pieces).reshape(-1, DH)
    moms = [(m_w_in, v_w_in), (m_w_br_hg, v_w_br_hg), (m_w_br_gla, v_w_br_gla), (m_w_out, v_w_out),
            (m_w_ff_gate, v_w_ff_gate), (m_w_ff_up, v_w_ff_up), (m_w_ff_down, v_w_ff_down)]
    names = ["w_in", "w_br_hg", "w_br_gla", "w_out", "w_ff_gate", "w_ff_up", "w_ff_down"]
    wmv = {nm: (w, m, v) for nm, w, (m, v) in zip(names, big, moms)}
    res = {}

    def update(nm):
        w, m, v = wmv[nm]
        if nm in ("w_ff_gate", "w_ff_up"):
            outs = _adamw(recv[nm], w, tr_(m), tr_(v), "adamw_" + nm)
            res[nm] = [jnp.swapaxes(o, 0, 1)[None] for o in outs]
        else:
            res[nm] = [o[None] for o in _adamw(recv[nm], w, m[0], v[0], "adamw_" + nm)]

    (small_handle, handle), tok = _split_start([whole([pack]), w_in_chunk(1)], "small_grads_start", pack)
    sent_w_in(1, handle)
    recv = {}
    for first, leaves, handle in sent:
        if not first.startswith("w_in"):
            recv.update(zip(leaves, _split_wait(handle, "grads_%s_wait" % first, tok)))
    update("w_ff_gate")
    update("w_ff_up")
    pack_all, = _split_wait(small_handle, "small_grads_wait", [res["w_ff_gate"][0], res["w_ff_up"][0]])
    tot = _sum_devices(pack_all).reshape(-1)
    offs = [sum(sizes[:i]) for i in range(len(sizes))]
    part = lambda i: tot[offs[i]:offs[i] + sizes[i]]
    dmodc_t, dmodx_t = part(0), part(1)
    g_b_mod = (dmodc_t + dmodx_t)[None]
    g_norms = [part(i)[None] for i in (2, 3, 4, 5)]
    g_hg_on, g_gla_on = part(6)[None], part(7)[None]
    lb0 = lax.dynamic_slice(part(8).reshape(2, HW), (0, me * (HW // N_DEV)), (2, HW // N_DEV))
    g_hg_lb = jnp.stack([lb0, -lb0])
    g_wgk = lax.dynamic_slice(part(9).reshape(2, 16, HW), (0, 0, me * (HW // N_DEV)), (2, 16, HW // N_DEV))[None]
    g_bgk = lax.dynamic_slice(part(10).reshape(2, HW), (0, me * (HW // N_DEV)), (2, HW // N_DEV))[None]
    loss = part(11)[0]

    dmx_all = pack_all.reshape(N_DEV, -1)[:, sizes[0]:sizes[0] + sizes[1]]
    d9 = jnp.concatenate([lax.dynamic_slice(dmodc_t[None], (0, me * n_mod), (1, n_mod)),
                          lax.dynamic_slice(dmx_all, (0, me * n_mod), (N_DEV, n_mod)),
                          jnp.zeros((16 - 1 - N_DEV, n_mod), F32)], axis=0)
    g_w_mod, dcc_part = _mod_bwd(a9, d9, w_mod[0])
    (cctx_handle, handle), tok = _split_start([whole([dcc_part]), w_in_chunk(2)], "c_ctx_start", dcc_part)
    sent_w_in(2, handle)
    recv["w_ff_down"] = _tie(recv["w_ff_down"], tok, "tie_down")
    update("w_ff_down")
    res["w_mod"] = [o[None] for o in _adamw(g_w_mod[None], w_mod[0], m_w_mod[0], v_w_mod[0], "adamw_w_mod")]
    for nm in ("w_out", "w_br_hg", "w_br_gla"):
        update(nm)
    dcc_all, = _split_wait(cctx_handle, "c_ctx_wait", [res["w_ff_down"][0], res["w_mod"][0]])
    g_c_ctx = _sum_devices(dcc_all)[0]

    small = [("c_ctx", c_ctx, m_c_ctx, v_c_ctx, g_c_ctx), ("b_mod", b_mod, m_b_mod, v_b_mod, g_b_mod),
             ("norm_pre1", norm_pre1, m_norm_pre1, v_norm_pre1, g_norms[0]),
             ("norm_post1", norm_post1, m_norm_post1, v_norm_post1, g_norms[1]),
             ("norm_pre2", norm_pre2, m_norm_pre2, v_norm_pre2, g_norms[2]),
             ("norm_post2", norm_post2, m_norm_post2, v_norm_post2, g_norms[3]),
             ("hg_lb", hg_lb, m_hg_lb, v_hg_lb, g_hg_lb), ("hg_onorm", hg_onorm, m_hg_onorm, v_hg_onorm, g_hg_on),
             ("gla_w_gk", gla_w_gk, m_gla_w_gk, v_gla_w_gk, g_wgk), ("gla_b_gk", gla_b_gk, m_gla_b_gk, v_gla_b_gk, g_bgk),
             ("gla_onorm", gla_onorm, m_gla_onorm, v_gla_onorm, g_gla_on)]
    flat = lambda k: jnp.concatenate([s[k].reshape(-1) for s in small]).reshape(-1, DH)
    outs = _adamw(flat(4)[None], flat(1), flat(2), flat(3), "adamw_small")
    off = 0
    for nm, w, _, _, _ in small:
        res[nm] = [o.reshape(-1)[off:off + w.size].reshape(w.shape) for o in outs]
        off += w.size

    done = [res[nm][0] for nm in names[1:]] + [res["w_mod"][0]] + [o for nm, *_ in small for o in res[nm]]
    major = lambda a: jnp.transpose(a, (2, 0, 1))
    outs, row0 = None, 0
    for i, (first, leaves, handle) in enumerate(s for s in sent if s[0].startswith("w_in")):
        half = W_IN_GRAD_CHUNKS[i][0]
        land, = _split_wait(handle, "grads_%s_wait" % first, done, srcs=w_in_grad[half])
        w_in_grad[half] = handle["srcs"]
        rows = (row0, row0 + land.shape[1])
        outs = _adamw_rows3(_sum_windows(land, "sum_windows%d" % i), major(w_in), major(m_w_in), major(v_w_in),
                            "adamw_w_in%d" % i, rows, outs)
        row0 = rows[1]
    res["w_in"] = [jnp.transpose(o, (1, 2, 0)) for o in outs]

    order = ["c_ctx", "w_mod", "b_mod", "norm_pre1", "norm_post1", "norm_pre2", "norm_post2", "w_in", "hg_lb",
             "hg_onorm", "gla_w_gk", "gla_b_gk", "gla_onorm", "w_br_hg", "w_br_gla", "w_out", "w_ff_gate", "w_ff_up",
             "w_ff_down"]
    return (loss, grad_x, *[res[n][k] for k in range(4) for n in order])
```

```python
import functools

import jax
import jax.numpy as jnp
from jax import lax
from jax.experimental import pallas as pl
from jax.experimental.pallas import tpu as pltpu

F32 = jnp.float32
BF16 = jnp.bfloat16
HI = lax.Precision.HIGHEST

N_DEV = 8
D = 1024
CTX = 256
HW = 512
DH = 128
NH = 8
D_FF = 2816
EPS = 1e-6
GLA_NORM = 16.0
CHUNK = 64
TR = 256
NCT = CTX // TR
W_IN_COLS = 7168
MAIN0 = 0
LR0 = 4608
GW = 1152
GOFF = 32
GATE_HG0 = LR0
GATE_GLA0 = LR0 + D
LEVELS = (32, 16, 8)
EXP_CLAMP = 80.0
VMEM_LIMIT = 48 * 1024 * 1024

ADAM_LR, ADAM_B1, ADAM_B2, ADAM_EPS, ADAM_WD, ADAM_STEP = 0.001, 0.9, 0.999, 1e-08, 0.01, 10


def _cp(*sem):
    return pltpu.CompilerParams(dimension_semantics=sem, vmem_limit_bytes=VMEM_LIMIT)


def _sig(x):
    return jax.nn.sigmoid(x)


def _silu(x):
    return x * _sig(x)


def _dsilu(x):
    s = _sig(x)
    return s * (1.0 + x * (1.0 - s))


def _rstd(x):
    return lax.rsqrt(jnp.mean(x * x, axis=-1, keepdims=True) + EPS)


def _rms_bwd(a, y, r):
    return r * (a - y * (r * r) * jnp.mean(a * y, axis=-1, keepdims=True))


def _colsum(x):
    return jnp.sum(x, axis=0, keepdims=True)


def _dot(a, b, dims, precision=None):
    return lax.dot_general(a, b, (dims, ((), ())), preferred_element_type=F32, precision=precision)


NN = ((1,), (0,))
NT = ((1,), (1,))
TN = ((0,), (0,))

SCAN_HEADS_FWD = 4
SCAN_HEADS_BWD = 4


def _split_dot(m, x):
    mb = m.astype(BF16)
    x1 = x.astype(BF16)
    r1 = x - x1.astype(F32)
    x2 = r1.astype(BF16)
    x3 = (r1 - x2.astype(F32)).astype(BF16)
    return _dot(mb, x1, NN) + _dot(mb, x2, NN) + _dot(mb, x3, NN)


def _matmul(a, b, dims, out_dtype, name, tm, tn, tk, a_off=0, m_out=None):
    a_pair = isinstance(a, (tuple, list))
    as_ = list(a) if a_pair else [a]
    a = as_[0]
    pair = isinstance(b, (tuple, list))
    bs = list(b) if pair else [b]
    b1 = bs[0]
    rows = b1.shape[0] * len(bs)
    half = None
    if dims == NN:
        m, k, n = a.shape[0], rows, b1.shape[1]
        a_spec = pl.BlockSpec((tm, tk), lambda i, j, kk: (i, kk + a_off))
        half = b1.shape[0] // tk
        if a_pair:
            assert pair and a.shape[1] == b1.shape[0] and a_off == 0
            a_spec = [pl.BlockSpec((tm, tk), lambda i, j, kk: (i, jnp.minimum(kk, half - 1))),
                      pl.BlockSpec((tm, tk), lambda i, j, kk: (i, jnp.maximum(kk - half, 0)))]
        b_maps = [lambda i, j, kk: (kk, j)] if not pair else [
            lambda i, j, kk: (jnp.minimum(kk, half - 1), j), lambda i, j, kk: (jnp.maximum(kk - half, 0), j)]
        b_specs = [pl.BlockSpec((tk, tn), f) for f in b_maps]
        axis = 2
    elif dims == NT:
        m, k, n = a.shape[0], b1.shape[1], rows
        a_spec = pl.BlockSpec((tm, tk), lambda i, j, kk: (i, kk + a_off))
        half = b1.shape[0] // tn
        b_maps = [lambda i, j, kk: (j, kk)] if not pair else [
            lambda i, j, kk: (jnp.minimum(j, half - 1), kk), lambda i, j, kk: (jnp.maximum(j - half, 0), kk)]
        b_specs = [pl.BlockSpec((tn, tk), f) for f in b_maps]
        axis = 1
    else:
        assert not pair
        m, k = (a.shape[1] if m_out is None else m_out), a.shape[0]
        n = b1.shape[1]
        a_spec = pl.BlockSpec((tk, tm), lambda i, j, kk: (kk, i + a_off))
        b_specs = [pl.BlockSpec((tk, tn), lambda i, j, kk: (kk, j))]
    assert m % tm == 0 and n % tn == 0 and k % tk == 0, (name, m, n, k, tm, tn, tk)
    nk = k // tk
    nb = len(bs)
    na = len(as_)
    assert na == 1 or dims == NN

    def body(*refs):
        a_refs, refs = refs[:na], refs[na:]
        o_ref = refs[nb]
        if pair:
            bv = jnp.where(pl.program_id(axis) < half, refs[0][...], refs[1][...])
        else:
            bv = refs[0][...]
        av = a_refs[0][...] if na == 1 else jnp.where(pl.program_id(2) < half, a_refs[0][...], a_refs[1][...])
        part = _dot(av, bv, dims)
        if nk == 1:
            o_ref[...] = part.astype(o_ref.dtype)
            return
        acc_ref = refs[nb + 1]
        kk = pl.program_id(2)

        @pl.when(kk == 0)
        def _():
            acc_ref[...] = part

        @pl.when(kk > 0)
        def _():
            acc_ref[...] += part

        @pl.when(kk == nk - 1)
        def _():
            o_ref[...] = acc_ref[...].astype(o_ref.dtype)

    return pl.pallas_call(
        body,
        name=name,
        grid=(m // tm, n // tn, nk),
        in_specs=(a_spec if a_pair else [a_spec]) + b_specs,
        out_specs=pl.BlockSpec((tm, tn), lambda i, j, kk: (i, j)),
        out_shape=jax.ShapeDtypeStruct((m, n), out_dtype),
        scratch_shapes=[] if nk == 1 else [pltpu.VMEM((tm, tn), F32)],
        compiler_params=_cp("parallel", "parallel", "arbitrary"),
    )(*as_, *bs)


def _mm_gu_act(h, w_gate_t, w_up_t, name, tm):
    t = h.shape[0]
    tn = D_FF // 2

    def body(a_ref, bg_ref, bu_ref, u_ref, v_ref, act_ref):
        a = a_ref[...]
        u = _dot(a, bg_ref[...], NT)
        v = _dot(a, bu_ref[...], NT)
        u_ref[...] = u.astype(BF16)
        v_ref[...] = v.astype(BF16)
        act_ref[...] = (_silu(u) * v).astype(BF16)

    wspec = pl.BlockSpec((tn, D), lambda i, j: (j, 0))
    ospec = pl.BlockSpec((tm, tn), lambda i, j: (i, j))
    out = jax.ShapeDtypeStruct((t, D_FF), BF16)
    return pl.pallas_call(
        body, name=name, grid=(t // tm, D_FF // tn),
        in_specs=[pl.BlockSpec((tm, D), lambda i, j: (i, 0)), wspec, wspec],
        out_specs=[ospec] * 3, out_shape=[out] * 3,
        compiler_params=_cp("parallel", "parallel"),
    )(h, w_gate_t, w_up_t)


def _mm_down_dx_act(dy, w_down, u, v, name, tm):
    t = dy.shape[0]
    tn = D_FF // 2

    def body(a_ref, b_ref, u_ref, v_ref, du_ref, dv_ref):
        dact = _dot(a_ref[...], b_ref[...], NT)
        u = u_ref[...].astype(F32)
        du_ref[...] = (dact * v_ref[...].astype(F32) * _dsilu(u)).astype(BF16)
        dv_ref[...] = (dact * _silu(u)).astype(BF16)

    ospec = pl.BlockSpec((tm, tn), lambda i, j: (i, j))
    out = jax.ShapeDtypeStruct((t, D_FF), BF16)
    return pl.pallas_call(
        body, name=name, grid=(t // tm, D_FF // tn),
        in_specs=[pl.BlockSpec((tm, D), lambda i, j: (i, 0)), pl.BlockSpec((tn, D), lambda i, j: (j, 0)), ospec, ospec],
        out_specs=[ospec] * 2, out_shape=[out] * 2,
        compiler_params=_cp("parallel", "parallel"),
    )(dy, w_down, u, v)


def _row(c):
    return pl.BlockSpec((TR, c), lambda i: (i, 0))


def _rowcol(width, cb):
    return pl.BlockSpec((TR, width), lambda i: (i, cb))


def _full(shape):
    return pl.BlockSpec(shape, lambda i: (0,) * len(shape))


def _mod_row(mc_ref, mx_ref, k, is_ctx):
    return jnp.where(is_ctx, mc_ref[k:k + 1, :], mx_ref[k:k + 1, :])


def _z_specs():
    return [pl.BlockSpec((TR, D), lambda i: (jnp.minimum(i, NCT - 1), 0)),
            pl.BlockSpec((TR, D), lambda i: (jnp.maximum(i - NCT, 0), 0))]


def _z_tile(c_ref, x_ref, is_ctx):
    return jnp.where(is_ctx, c_ref[...], x_ref[...])


def _acc_row(ref, k, val):
    ref[k:k + 1, :] += val


def _acc_mod(ref, k, is_ctx, val):
    zero = jnp.zeros_like(val)
    ref[k:k + 1, :] += jnp.where(is_ctx, val, zero)
    ref[k + 1:k + 2, :] += jnp.where(is_ctx, zero, val)


def _prenorm(z, nw, modc, modx, i_shift, i_scale, name):
    t = z[0].shape[0] + z[1].shape[0]

    def body(zc_ref, zx_ref, nw_ref, mc_ref, mx_ref, h_ref):
        is_ctx = pl.program_id(0) < NCT
        x = _z_tile(zc_ref, zx_ref, is_ctx)
        n = x * _rstd(x) * nw_ref[...]
        h = n * (1.0 + _mod_row(mc_ref, mx_ref, i_scale, is_ctx)) + _mod_row(mc_ref, mx_ref, i_shift, is_ctx)
        h_ref[...] = h.astype(BF16)

    return pl.pallas_call(
        body, name=name, grid=(t // TR,),
        in_specs=_z_specs() + [_full((1, D)), _full((8, D)), _full((8, D))],
        out_specs=_row(D),
        out_shape=jax.ShapeDtypeStruct((t, D), BF16),
        compiler_params=_cp("parallel"),
    )(*z, nw, modc, modx)


def _hg_lb(lb_ref, d):
    a0 = lb_ref[0, d:d + 1, :]
    a1 = lb_ref[1, d:d + 1, :]
    mx = jnp.maximum(a0, a1)
    e0 = jnp.exp(a0 - mx)
    e1 = jnp.exp(a1 - mx)
    return e0 / (e0 + e1)


def _log_sigmoid(x):
    return jnp.minimum(x, 0.0) - jnp.log(1.0 + jnp.exp(-jnp.abs(x)))


def _gates_fwd(p, hg_lb, wgk, bgk):
    t = p.shape[0]
    seg = lambda j: _rowcol(HW, MAIN0 // HW + j)

    def body(hq_ref, hi_ref, hf_ref, hb_ref, gq_ref, gk_ref, gv_ref, lr_ref, lb_ref, wgk_ref, bgk_ref,
             q_ref, v_ref, kf_ref, kb_ref, gf_ref, gb_ref):
        q_ref[:, :HW] = _silu(hq_ref[...].astype(F32)).astype(BF16)
        q_ref[:, HW:] = (gq_ref[...].astype(F32) * (DH ** -0.5)).astype(BF16)
        v_ref[:, :HW] = hi_ref[...]
        v_ref[:, HW:] = gv_ref[...]
        xg = _dot(lr_ref[...].astype(BF16), wgk_ref[...], NN) + bgk_ref[...]
        for d, (raw_ref, k_ref, g_ref) in enumerate(((hf_ref, kf_ref, gf_ref), (hb_ref, kb_ref, gb_ref))):
            lbd = _hg_lb(lb_ref, d)
            f = lbd + (1.0 - lbd) * _sig(raw_ref[...].astype(F32))
            k_ref[:, :HW] = (1.0 - f).astype(BF16)
            k_ref[:, HW:] = gk_ref[...]
            g_ref[:, :HW] = jnp.log(f)
            g_ref[:, HW:] = _log_sigmoid(xg[:, d * HW:(d + 1) * HW]) * (1.0 / GLA_NORM)

    out = jax.ShapeDtypeStruct((t, D), F32)
    outb = jax.ShapeDtypeStruct((t, D), BF16)
    return pl.pallas_call(
        body, name="gates_fwd", grid=(t // TR,),
        in_specs=[seg(0), seg(1), seg(2), seg(3), seg(5), seg(6), seg(7), _rowcol(DH, LR0 // DH),
                  _full((2, 2, HW)), _full((DH, D)), _full((1, D))],
        out_specs=[_row(D)] * 6,
        out_shape=[outb] * 4 + [out] * 2,
        compiler_params=_cp("parallel"),
    )(p, p, p, p, p, p, p, p, hg_lb, wgk, bgk)


def _post_fwd(o_fw, o_bw, p, onw):
    t = o_fw.shape[0]

    def body(of_ref, ob_ref, g1_ref, g2_ref, w_ref, y_ref):
        for h in range(NH):
            sl = slice(h * DH, (h + 1) * DH)
            o = of_ref[:, sl] + ob_ref[:, sl]
            g_ref = g1_ref if h < NH // 2 else g2_ref
            gs = slice((h % (NH // 2)) * DH, (h % (NH // 2) + 1) * DH)
            n = o * _rstd(o) * w_ref[:, sl]
            y_ref[:, sl] = (n * _silu(g_ref[:, gs].astype(F32))).astype(BF16)

    return pl.pallas_call(
        body, name="post_fwd", grid=(t // TR,),
        in_specs=[_row(D), _row(D), _rowcol(HW, MAIN0 // HW + 4), _rowcol(HW, MAIN0 // HW + 8), _full((1, D))],
        out_specs=_row(D),
        out_shape=jax.ShapeDtypeStruct((t, D), BF16),
        compiler_params=_cp("parallel"),
    )(o_fw, o_bw, p, p, onw)


def _gate_window_specs(col0):
    return [_rowcol(HW, col0 // HW), _rowcol(HW, col0 // HW + 1), _rowcol(DH, (col0 + 2 * HW) // DH)]


def _gate_window(refs):
    return jnp.concatenate([r[...].astype(F32) for r in refs], axis=1)


def _branch_merge(y, w_hg, w_gla, p):
    t = y.shape[0]

    def body(y_ref, wh_ref, wg_ref, a0, a1, a2, b0, b1, b2, u1_ref, u2_ref, m_ref):
        u1 = _dot(y_ref[:, :HW], wh_ref[...], NN)
        u2 = _dot(y_ref[:, HW:], wg_ref[...], NN)
        u1_ref[...] = u1.astype(BF16)
        u2_ref[...] = u2.astype(BF16)
        m_ref[...] = (_sig(_gate_window((a0, a1, a2))) * u1 + _sig(_gate_window((b0, b1, b2))) * u2).astype(BF16)

    out = jax.ShapeDtypeStruct((t, GW), BF16)
    return pl.pallas_call(
        body, name="branch_merge", grid=(t // TR,),
        in_specs=[_row(D), _full((HW, GW)), _full((HW, GW))] + _gate_window_specs(GATE_HG0)
        + _gate_window_specs(GATE_GLA0),
        out_specs=[_row(GW)] * 3, out_shape=[out] * 3,
        compiler_params=_cp("parallel"),
    )(y, w_hg, w_gla, p, p, p, p, p, p)


def _mid_fwd(z, y1, nw_post, nw_pre, modc, modx):
    t = y1.shape[0]

    def body(zc_ref, zx_ref, y_ref, wpo_ref, wpr_ref, mc_ref, mx_ref, z1_ref, h_ref):
        is_ctx = pl.program_id(0) < NCT
        y = y_ref[...].astype(F32)
        z1 = _z_tile(zc_ref, zx_ref, is_ctx) + _mod_row(mc_ref, mx_ref, 2, is_ctx) * (y * _rstd(y) * wpo_ref[...])
        z1_ref[...] = z1
        n = z1 * _rstd(z1) * wpr_ref[...]
        h = n * (1.0 + _mod_row(mc_ref, mx_ref, 4, is_ctx)) + _mod_row(mc_ref, mx_ref, 3, is_ctx)
        h_ref[...] = h.astype(BF16)

    return pl.pallas_call(
        body, name="mid_fwd", grid=(t // TR,),
        in_specs=_z_specs() + [_row(D), _full((1, D)), _full((1, D)), _full((8, D)), _full((8, D))],
        out_specs=[_row(D), _row(D)],
        out_shape=[jax.ShapeDtypeStruct((t, D), F32), jax.ShapeDtypeStruct((t, D), BF16)],
        compiler_params=_cp("parallel"),
    )(*z, y1, nw_post, nw_pre, modc, modx)


def _final(z1, y2, target, nw, modc, modx):
    t = z1.shape[0]

    def body(z1_ref, y_ref, tg_ref, w_ref, mc_ref, mx_ref, dz_ref, dy_ref, loss_ref, sm_ref):
        i = pl.program_id(0)
        is_ctx = i < NCT

        @pl.when(i == 0)
        def _():
            loss_ref[...] = jnp.zeros_like(loss_ref)
            sm_ref[...] = jnp.zeros_like(sm_ref)

        g = _mod_row(mc_ref, mx_ref, 5, is_ctx)
        y = y_ref[...].astype(F32)
        r = _rstd(y)
        w = w_ref[...]
        yr = y * r
        n = yr * w
        e = z1_ref[...] + g * n - tg_ref[...]
        lat = jnp.where(is_ctx, 0.0, 1.0)
        loss_ref[...] += lat * _colsum(e * e)
        dz = e * (lat / D)
        dz_ref[...] = dz
        _acc_mod(sm_ref, 0, is_ctx, _colsum(dz * n))
        dn = dz * g
        _acc_row(sm_ref, 2, _colsum(dn * yr))
        dy_ref[...] = _rms_bwd(dn * w, y, r).astype(BF16)

    return pl.pallas_call(
        body, name="final", grid=(t // TR,),
        in_specs=[_row(D), _row(D), pl.BlockSpec((TR, D), lambda i: (jnp.maximum(i - NCT, 0), 0)),
                  _full((1, D)), _full((8, D)), _full((8, D))],
        out_specs=[_row(D), _row(D), _full((1, D)), _full((8, D))],
        out_shape=[jax.ShapeDtypeStruct((t, D), F32), jax.ShapeDtypeStruct((t, D), BF16),
                   jax.ShapeDtypeStruct((1, D), F32), jax.ShapeDtypeStruct((8, D), F32)],
        compiler_params=_cp("arbitrary"),
    )(z1, y2, target, nw, modc, modx)


def _mid_bwd(dh2, dz, z1, y1, nw_post, nw_pre, modc, modx):
    t = z1.shape[0]

    def body(dh_ref, dz_ref, z1_ref, y_ref, wpo_ref, wpr_ref, mc_ref, mx_ref, dzo_ref, dy_ref, sm_ref):
        i = pl.program_id(0)
        is_ctx = i < NCT

        @pl.when(i == 0)
        def _():
            sm_ref[...] = jnp.zeros_like(sm_ref)

        dh = dh_ref[...].astype(F32)
        z1 = z1_ref[...]
        r = _rstd(z1)
        zr = z1 * r
        wpr = wpr_ref[...]
        n = zr * wpr
        _acc_mod(sm_ref, 0, is_ctx, _colsum(dh))
        _acc_mod(sm_ref, 2, is_ctx, _colsum(dh * n))
        dn = dh * (1.0 + _mod_row(mc_ref, mx_ref, 4, is_ctx))
        _acc_row(sm_ref, 6, _colsum(dn * zr))
        dz1 = dz_ref[...] + _rms_bwd(dn * wpr, z1, r)
        dzo_ref[...] = dz1
        y = y_ref[...].astype(F32)
        r1 = _rstd(y)
        yr = y * r1
        wpo = wpo_ref[...]
        g = _mod_row(mc_ref, mx_ref, 2, is_ctx)
        _acc_mod(sm_ref, 4, is_ctx, _colsum(dz1 * (yr * wpo)))
        dn1 = dz1 * g
        _acc_row(sm_ref, 7, _colsum(dn1 * yr))
        dy_ref[...] = _rms_bwd(dn1 * wpo, y, r1).astype(BF16)

    return pl.pallas_call(
        body, name="mid_bwd", grid=(t // TR,),
        in_specs=[_row(D)] * 4 + [_full((1, D)), _full((1, D)), _full((8, D)), _full((8, D))],
        out_specs=[_row(D), _row(D), _full((8, D))],
        out_shape=[jax.ShapeDtypeStruct((t, D), F32), jax.ShapeDtypeStruct((t, D), BF16),
                   jax.ShapeDtypeStruct((8, D), F32)],
        compiler_params=_cp("arbitrary"),
    )(dh2, dz, z1, y1, nw_post, nw_pre, modc, modx)


def _pre_bwd(dh1, dz, z, nw, modc, modx):
    t = dh1.shape[0]

    def body(dh_ref, dz_ref, zc_ref, zx_ref, w_ref, mc_ref, mx_ref, dzo_ref, sm_ref):
        i = pl.program_id(0)
        is_ctx = i < NCT

        @pl.when(i == 0)
        def _():
            sm_ref[...] = jnp.zeros_like(sm_ref)

        dh = dh_ref[...].astype(F32)
        x = _z_tile(zc_ref, zx_ref, is_ctx)
        r = _rstd(x)
        xr = x * r
        w = w_ref[...]
        _acc_mod(sm_ref, 0, is_ctx, _colsum(dh))
        _acc_mod(sm_ref, 2, is_ctx, _colsum(dh * (xr * w)))
        dn = dh * (1.0 + _mod_row(mc_ref, mx_ref, 1, is_ctx))
        _acc_row(sm_ref, 4, _colsum(dn * xr))
        dzo_ref[...] = dz_ref[...] + _rms_bwd(dn * w, x, r)

    return pl.pallas_call(
        body, name="pre_bwd", grid=(t // TR,),
        in_specs=[_row(D)] * 2 + _z_specs() + [_full((1, D)), _full((8, D)), _full((8, D))],
        out_specs=[pl.BlockSpec((TR, D), lambda i: (jnp.maximum(i - NCT, 0), 0)), _full((8, D))],
        out_shape=[jax.ShapeDtypeStruct((t - CTX, D), F32), jax.ShapeDtypeStruct((8, D), F32)],
        compiler_params=_cp("arbitrary"),
    )(dh1, dz, *z, nw, modc, modx)


def _branch_merge_bwd(dm, p, u1, u2, w_hg, w_gla):
    t = dm.shape[0]

    def body(dm_ref, a0, a1, a2, b0, b1, b2, u1_ref, u2_ref, wh_ref, wg_ref, du1_ref, du2_ref, dg_ref, dyh_ref, dyg_ref):
        dm_ = dm_ref[...].astype(F32)
        s1 = _sig(_gate_window((a0, a1, a2)))
        s2 = _sig(_gate_window((b0, b1, b2)))
        du1 = (dm_ * s1).astype(BF16)
        du2 = (dm_ * s2).astype(BF16)
        du1_ref[...] = du1
        du2_ref[...] = du2
        dg_ref[:, :GW] = (dm_ * u1_ref[...].astype(F32) * s1 * (1.0 - s1)).astype(BF16)
        dg_ref[:, GW:] = (dm_ * u2_ref[...].astype(F32) * s2 * (1.0 - s2)).astype(BF16)
        dyh_ref[...] = _dot(du1, wh_ref[...], NT).astype(BF16)
        dyg_ref[...] = _dot(du2, wg_ref[...], NT).astype(BF16)

    return pl.pallas_call(
        body, name="branch_merge_bwd", grid=(t // TR,),
        in_specs=[_row(GW)] + _gate_window_specs(GATE_HG0) + _gate_window_specs(GATE_GLA0)
        + [_row(GW), _row(GW), _full((HW, GW)), _full((HW, GW))],
        out_specs=[_row(GW), _row(GW), _row(2 * GW), _row(HW), _row(HW)],
        out_shape=[jax.ShapeDtypeStruct((t, GW), BF16), jax.ShapeDtypeStruct((t, GW), BF16),
                   jax.ShapeDtypeStruct((t, 2 * GW), BF16), jax.ShapeDtypeStruct((t, HW), BF16),
                   jax.ShapeDtypeStruct((t, HW), BF16)],
        compiler_params=_cp("parallel"),
    )(dm, p, p, p, p, p, p, u1, u2, w_hg, w_gla)


def _post_bwd(dy_hg, dy_gla, o_fw, o_bw, p, onw):
    t = o_fw.shape[0]

    def body(d1_ref, d2_ref, of_ref, ob_ref, g1_ref, g2_ref, w_ref, do_ref, dg_ref, sm_ref):
        @pl.when(pl.program_id(0) == 0)
        def _():
            sm_ref[...] = jnp.zeros_like(sm_ref)

        for h in range(NH):
            sl = slice(h * DH, (h + 1) * DH)
            gs = slice((h % (NH // 2)) * DH, (h % (NH // 2) + 1) * DH)
            g_ref, d_ref = (g1_ref, d1_ref) if h < NH // 2 else (g2_ref, d2_ref)
            o = of_ref[:, sl] + ob_ref[:, sl]
            r = _rstd(o)
            orr = o * r
            w = w_ref[:, sl]
            gt = g_ref[:, gs].astype(F32)
            dy = d_ref[:, gs].astype(F32)
            dg_ref[:, sl] = (dy * (orr * w) * _dsilu(gt)).astype(BF16)
            dn = dy * _silu(gt)
            sm_ref[0:1, sl] += _colsum(dn * orr)
            do_ref[:, sl] = _rms_bwd(dn * w, o, r)

    return pl.pallas_call(
        body, name="post_bwd", grid=(t // TR,),
        in_specs=[_row(HW), _row(HW), _row(D), _row(D), _rowcol(HW, MAIN0 // HW + 4), _rowcol(HW, MAIN0 // HW + 8),
                  _full((1, D))],
        out_specs=[_row(D), _row(D), _full((8, D))],
        out_shape=[jax.ShapeDtypeStruct((t, D), F32), jax.ShapeDtypeStruct((t, D), BF16),
                   jax.ShapeDtypeStruct((8, D), F32)],
        compiler_params=_cp("arbitrary"),
    )(dy_hg, dy_gla, o_fw, o_bw, p, p, onw)


def _gates_bwd(p, hg_lb, wgk, bgk, dgm, dgo, dq_f, dq_b, dv_f, dv_b, dk_f, dk_b, dg_f, dg_b):
    t = p.shape[0]
    seg = lambda j: _rowcol(HW, MAIN0 // HW + j)

    def body(hq_ref, hf_ref, hb_ref, lr_ref, lb_ref, wgk_ref, bgk_ref, dgm_ref, dgo_ref,
             dqf_ref, dqb_ref, dvf_ref, dvb_ref, dkf_ref, dkb_ref, dgf_ref, dgb_ref,
             dp_ref, dlb_ref, dw_ref, db_ref):
        @pl.when(pl.program_id(0) == 0)
        def _():
            dlb_ref[...] = jnp.zeros_like(dlb_ref)
            dw_ref[...] = jnp.zeros_like(dw_ref)
            db_ref[...] = jnp.zeros_like(db_ref)

        c0 = MAIN0

        def put(j, val):
            dp_ref[:, c0 + j * HW:c0 + (j + 1) * HW] = val.astype(BF16)

        dq = dqf_ref[...].astype(F32) + dqb_ref[...].astype(F32)
        dv = dvf_ref[...].astype(F32) + dvb_ref[...].astype(F32)
        put(0, dq[:, :HW] * _dsilu(hq_ref[...].astype(F32)))
        put(1, dv[:, :HW])
        put(5, dq[:, HW:] * (DH ** -0.5))
        put(7, dv[:, HW:])
        put(6, dkf_ref[:, HW:].astype(F32) + dkb_ref[:, HW:].astype(F32))
        dp_ref[:, c0 + 4 * HW:c0 + 5 * HW] = dgo_ref[:, :HW]
        dp_ref[:, c0 + 8 * HW:c0 + 9 * HW] = dgo_ref[:, HW:]
        lr = lr_ref[...].astype(BF16)
        xg = _dot(lr, wgk_ref[...], NN) + bgk_ref[...]
        dxg = []
        for d, (raw_ref, dk_ref, dg_ref) in enumerate(((hf_ref, dkf_ref, dgf_ref), (hb_ref, dkb_ref, dgb_ref))):
            lbd = _hg_lb(lb_ref, d)
            s = _sig(raw_ref[...].astype(F32))
            f = lbd + (1.0 - lbd) * s
            df = dg_ref[:, :HW] / f - dk_ref[:, :HW].astype(F32)
            put(2 + d, df * (1.0 - lbd) * s * (1.0 - s))
            dlb_ref[d:d + 1, :] += _colsum(df * (1.0 - s)) * (lbd * (1.0 - lbd))
            dxg.append(dg_ref[:, HW:] * (1.0 / GLA_NORM) * _sig(-xg[:, d * HW:(d + 1) * HW]))
        dxg = jnp.concatenate(dxg, axis=1)
        db_ref[0:1, :] += _colsum(dxg)
        dxg_b = dxg.astype(BF16)
        dw_ref[...] += _dot(lr, dxg_b, TN)
        dlr = _dot(dxg_b, wgk_ref[...], NT)
        dp_ref[:, LR0:LR0 + DH] = (dlr + dgm_ref[:, :DH].astype(F32)).astype(BF16)
        dp_ref[:, LR0 + DH:GATE_GLA0] = dgm_ref[:, DH:D]
        dp_ref[:, GATE_GLA0:GATE_GLA0 + DH] = dgm_ref[:, D:GW] + dgm_ref[:, GW:GW + DH]
        dp_ref[:, GATE_GLA0 + DH:GATE_GLA0 + GW] = dgm_ref[:, GW + DH:]
        dp_ref[:, GATE_GLA0 + GW:] = jnp.zeros((TR, W_IN_COLS - GATE_GLA0 - GW), BF16)

    return pl.pallas_call(
        body, name="gates_bwd", grid=(t // TR,),
        in_specs=[seg(0), seg(2), seg(3), _rowcol(DH, LR0 // DH), _full((2, 2, HW)), _full((DH, D)), _full((1, D)),
                  _row(2 * GW), _row(D)] + [_row(D)] * 8,
        out_specs=[_row(W_IN_COLS), _full((8, HW)), _full((DH, D)), _full((8, D))],
        out_shape=[jax.ShapeDtypeStruct((t, W_IN_COLS), BF16), jax.ShapeDtypeStruct((8, HW), F32),
                   jax.ShapeDtypeStruct((DH, D), F32), jax.ShapeDtypeStruct((8, D), F32)],
        compiler_params=_cp("arbitrary"),
    )(p, p, p, p, hg_lb, wgk, bgk, dgm, dgo, dq_f, dq_b, dv_f, dv_b, dk_f, dk_b, dg_f, dg_b)


def _scan_consts(rev):
    r = lax.broadcasted_iota(jnp.int32, (CHUNK, CHUNK), 0)
    u = lax.broadcasted_iota(jnp.int32, (CHUNK, CHUNK), 1)
    rp = lax.broadcasted_iota(jnp.int32, (CHUNK, 1), 0)
    if rev:
        r, u, rp = CHUNK - 1 - r, CHUNK - 1 - u, CHUNK - 1 - rp
    tri = jnp.where(u <= r, 1.0, 0.0).astype(F32)
    tri_t = jnp.where(r <= u, 1.0, 0.0).astype(F32)
    lv = []
    for b in LEVELS:
        sh = b.bit_length() - 1
        pair = ((r >> sh) == (u >> sh) + 1) & (((u >> sh) & 1) == 0)
        pair_t = ((u >> sh) == (r >> sh) + 1) & (((r >> sh) & 1) == 0)
        tside = ((rp >> sh) & 1) == 1
        lv.append((pair, pair_t, tside, jnp.where(tside, 1.0, -1.0).astype(F32)))
    bd = LEVELS[-1].bit_length() - 1
    diag = ((r >> bd) == (u >> bd)) & (u <= r)
    diag_t = ((r >> bd) == (u >> bd)) & (r <= u)
    return tri, tri_t, lv, diag, diag_t


def _row_of(pos, rev):
    return CHUNK - 1 - pos if rev else pos


def _chunk_terms(cum, b_scr, consts, rev):
    _, _, lv, _, _ = consts
    terms = []
    for b, (_, _, _, sgn) in zip(LEVELS, lv):
        pieces = []
        for j in range(CHUNK // (2 * b)):
            row = _row_of(2 * b * j + b - 1, rev)
            pieces.append(jnp.broadcast_to(b_scr[row:row + 1, :], (2 * b, DH)))
        if rev:
            pieces = pieces[::-1]
        bnd = pieces[0] if len(pieces) == 1 else jnp.concatenate(pieces, axis=0)
        terms.append(jnp.exp((cum - bnd) * sgn))
    b = LEVELS[-1]
    pieces = []
    for j in range(CHUNK // b):
        if j == 0:
            pieces.append(jnp.zeros((b, DH), F32))
        else:
            row = _row_of(b * j - 1, rev)
            pieces.append(jnp.broadcast_to(b_scr[row:row + 1, :], (b, DH)))
    if rev:
        pieces = pieces[::-1]
    start = jnp.concatenate(pieces, axis=0)
    wq = jnp.exp(jnp.minimum(cum - start, 0.0))
    wk = jnp.exp(jnp.minimum(start - cum, EXP_CLAMP))
    terms.append((wq, wk))
    return terms


def _run_staged(units):
    live = list(units)
    while live:
        nxt = []
        for u in live:
            try:
                next(u)
                nxt.append(u)
            except StopIteration:
                pass
        live = nxt


SCAN_TB = 256
SCAN_CB = SCAN_TB // CHUNK


def _block_order(i, ntb, rev):
    nctx = CTX // SCAN_TB
    if not rev:
        return i
    return jnp.where(i < nctx, nctx - 1 - i, ntb - 1 - (i - nctx))


def _chunk_in_block(j, rev):
    return SCAN_CB - 1 - j if rev else j


def _scan_fwd(q, k, v, g, rev):
    t = q.shape[0]
    nc = t // CHUNK
    hpb = SCAN_HEADS_FWD

    def body(q_ref, k_ref, v_ref, g_ref, o_ref, st_ref, s_scr, b_scr):
        consts = _scan_consts(rev)
        _, _, lv, diag, _ = consts
        masks = [lvl[0] for lvl in lv] + [diag]

        @pl.when(pl.program_id(1) == 0)
        def _():
            s_scr[...] = jnp.zeros_like(s_scr)

        tri = consts[0]
        state = {hh: s_scr[hh] for hh in range(hpb)}

        def unit(hh, j):
            sl = slice(hh * DH, (hh + 1) * DH)
            c = _chunk_in_block(j, rev)
            rows = slice(c * CHUNK, (c + 1) * CHUNK)
            b_ref = b_scr.at[hh * SCAN_CB + j]
            qc, kc, vc, gc = q_ref[rows, sl], k_ref[rows, sl], v_ref[rows, sl], g_ref[rows, sl]
            cum = _split_dot(tri, gc)
            b_ref[...] = cum
            yield
            terms = _chunk_terms(cum, b_ref, consts, rev)
            qf, kf = qc.astype(F32), kc.astype(F32)
            xs = [(jnp.where(tside, qf, kf) * w).astype(BF16) for w, (_, _, tside, _) in zip(terms[:-1], lv)]
            qd, kd = (qf * terms[-1][0]).astype(BF16), (kf * terms[-1][1]).astype(BF16)
            tot = _colsum(gc)
            qe = (qf * jnp.exp(cum)).astype(BF16)
            ke = (kf * jnp.exp(tot - cum)).astype(BF16)
            vb = vc.astype(BF16)
            yield
            scs = [_dot(x, x, NT) for x in xs] + [_dot(qd, kd, NT)]
            kv = _dot(vb, ke, TN)
            yield
            a = jnp.zeros((CHUNK, CHUNK), F32)
            for sc, m in zip(scs, masks):
                a = a + jnp.where(m, sc, 0.0)
            o_intra = _dot(a.astype(BF16), vb, NN)
            yield
            st = state[hh]
            st_ref[hh, c] = st
            o_ref[rows, sl] = o_intra + _dot(qe, st.astype(BF16), NT)
            state[hh] = st * jnp.exp(tot) + kv
            yield

        _run_staged([unit(hh, j) for hh in range(hpb) for j in range(SCAN_CB)])
        for hh in range(hpb):
            s_scr[hh] = state[hh]

    ntb = t // SCAN_TB
    col = pl.BlockSpec((SCAN_TB, hpb * DH), lambda h, i: (_block_order(i, ntb, rev), h))
    return pl.pallas_call(
        body, name="scan_fwd_" + ("bw" if rev else "fw"), grid=(NH // hpb, ntb),
        in_specs=[col] * 4,
        out_specs=[col, pl.BlockSpec((hpb, SCAN_CB, DH, DH), lambda h, i: (h, _block_order(i, ntb, rev), 0, 0))],
        out_shape=[jax.ShapeDtypeStruct((t, D), F32), jax.ShapeDtypeStruct((NH, nc, DH, DH), F32)],
        scratch_shapes=[pltpu.VMEM((hpb, DH, DH), F32), pltpu.VMEM((hpb * SCAN_CB, CHUNK, DH), F32)],
        compiler_params=_cp("parallel", "arbitrary"),
    )(q, k, v, g)


def _scan_bwd(q, k, v, g, do, states, rev):
    t = q.shape[0]
    nc = t // CHUNK
    hpb = SCAN_HEADS_BWD

    def body(q_ref, k_ref, v_ref, g_ref, do_ref, st_ref, dq_ref, dk_ref, dv_ref, dg_ref, ds_scr, b_scr):
        consts = _scan_consts(rev)
        _, tri_t, lv, diag, diag_t = consts
        masks = [(lvl[0], lvl[1]) for lvl in lv] + [(diag, diag_t)]
        @pl.when(pl.program_id(1) == 0)
        def _():
            ds_scr[...] = jnp.zeros_like(ds_scr)

        tri = consts[0]
        dstate = {hh: ds_scr[hh] for hh in range(hpb)}

        def unit(hh, jj):
            sl = slice(hh * DH, (hh + 1) * DH)
            c = _chunk_in_block(SCAN_CB - 1 - jj, rev)
            rows = slice(c * CHUNK, (c + 1) * CHUNK)
            b_ref = b_scr.at[hh * SCAN_CB + jj]
            qc, kc, vc, gc = q_ref[rows, sl], k_ref[rows, sl], v_ref[rows, sl], g_ref[rows, sl]
            dob = do_ref[rows, sl].astype(BF16)
            vb = vc.astype(BF16)
            cum = _split_dot(tri, gc)
            b_ref[...] = cum
            da = _dot(dob, vb, NT)
            da_t = _dot(vb, dob, NT)
            yield
            terms = _chunk_terms(cum, b_ref, consts, rev)
            qf, kf = qc.astype(F32), kc.astype(F32)
            xs = [(jnp.where(tside, qf, kf) * w).astype(BF16) for w, (_, _, tside, _) in zip(terms[:-1], lv)]
            wqd, wkd = terms[-1]
            qdb, kdb = (qf * wqd).astype(BF16), (kf * wkd).astype(BF16)
            tot = _colsum(gc)
            e_tot = jnp.exp(tot)
            e_b = jnp.exp(cum)
            e_t = jnp.exp(tot - cum)
            qeb = (qf * e_b).astype(BF16)
            keb = (kf * e_t).astype(BF16)
            dsym = [(jnp.where(m, da, 0.0) + jnp.where(m_t, da_t, 0.0)).astype(BF16) for m, m_t in masks[:-1]]
            dad = (jnp.where(diag, da, 0.0).astype(BF16), jnp.where(diag_t, da_t, 0.0).astype(BF16))
            yield
            sym = [_dot(x, x, NT) for x in xs]
            dxs = [_dot(d, x, NN) for d, x in zip(dsym, xs)]
            at_d = _dot(kdb, qdb, NT)
            dqt_d = _dot(dad[0], kdb, NN)
            dkt_d = _dot(dad[1], qdb, NN)
            qd = _dot(dob, qeb, TN)
            yield
            a_t = jnp.where(diag_t, at_d, 0.0)
            dq = dqt_d * wqd
            dk = dkt_d * wkd
            db = dqt_d * qdb.astype(F32) - dkt_d * kdb.astype(F32)
            for s, dx, x, w, (_, m_t, tside, sgn) in zip(sym, dxs, xs, terms[:-1], lv):
                a_t = a_t + jnp.where(m_t, s, 0.0)
                dxw = dx * w
                dq = dq + jnp.where(tside, dxw, 0.0)
                dk = dk + jnp.where(tside, 0.0, dxw)
                db = db + (dx * x.astype(F32)) * sgn
            dv_intra = _dot(a_t.astype(BF16), dob, NN)
            st = st_ref[hh, c]
            stb = st.astype(BF16)
            dqe = _dot(dob, stb, NN)
            yield
            dst = dstate[hh]
            dstb = dst.astype(BF16)
            dstate[hh] = dst * e_tot + qd
            dv_ref[rows, sl] = (dv_intra + _dot(keb, dstb, NT)).astype(BF16)
            dke = _dot(vb, dstb, NN)
            yield
            qe = qeb.astype(F32)
            ke = keb.astype(F32)
            dq_ref[rows, sl] = (dq + dqe * e_b).astype(BF16)
            dk_ref[rows, sl] = (dk + dke * e_t).astype(BF16)
            db = db + dqe * qe - dke * ke
            dtot = _colsum(dstb.astype(F32) * stb.astype(F32)) * e_tot + _colsum(dke * ke)
            dg_ref[rows, sl] = _split_dot(tri_t, db) + dtot
            yield

        _run_staged([unit(hh, jj) for hh in range(hpb) for jj in range(SCAN_CB)])
        for hh in range(hpb):
            ds_scr[hh] = dstate[hh]

    ntb = t // SCAN_TB
    blk = lambda i: _block_order(ntb - 1 - i, ntb, rev)
    col = pl.BlockSpec((SCAN_TB, hpb * DH), lambda h, i: (blk(i), h))
    out = jax.ShapeDtypeStruct((t, D), F32)
    outb = jax.ShapeDtypeStruct((t, D), BF16)
    return pl.pallas_call(
        body, name="scan_bwd_" + ("bw" if rev else "fw"), grid=(NH // hpb, ntb),
        in_specs=[col] * 5 + [pl.BlockSpec((hpb, SCAN_CB, DH, DH), lambda h, i: (h, blk(i), 0, 0))],
        out_specs=[col] * 4,
        out_shape=[outb] * 3 + [out],
        scratch_shapes=[pltpu.VMEM((hpb, DH, DH), F32), pltpu.VMEM((hpb * SCAN_CB, CHUNK, DH), F32)],
        compiler_params=_cp("parallel", "arbitrary"),
    )(q, k, v, g, do, states)


W_IN_GRAD_CHUNKS = (("a", (0, 512)), ("b", (0, 256)), ("b", (256, 512)))
W_IN_REF = 6688
W_IN_PAD = 896
W_IN_PIECE = 256
W_IN_STAGES = (3, 4)


def _assemble_w_in(g, rows, prev, name):
    n, r, wp = g.shape
    tr = W_IN_PIECE
    tiles = wp // DH
    first = rows[0] // tr

    def body(g_ref, *refs):
        o_ref = refs[-1]
        lane = lax.broadcasted_iota(jnp.int32, (tr, DH), 1)
        for t in range(W_IN_COLS // DH):
            acc = None
            for j in range(n):
                c = DH * t - W_IN_SHARD * j
                if c <= -DH or c >= W_IN_SHARD:
                    continue
                k, s = divmod(c, DH)
                lo = g_ref[j, :, k * DH:(k + 1) * DH] if 0 <= k < tiles else None
                hi = g_ref[j, :, (k + 1) * DH:(k + 2) * DH] if s and 0 <= k + 1 < tiles else None
                if s:
                    zero = jnp.zeros((tr, DH), g.dtype)
                    lo = zero if lo is None else pltpu.roll(lo, DH - s, 1)
                    hi = zero if hi is None else pltpu.roll(hi, DH - s, 1)
                    part = jnp.where(lane < DH - s, lo, hi)
                else:
                    part = lo
                acc = part if acc is None else acc + part
            o_ref[:, t * DH:(t + 1) * DH] = jnp.zeros((tr, DH), g.dtype) if acc is None else acc

    held = [] if prev is None else [prev]
    return pl.pallas_call(
        body, name=name, grid=((rows[1] - rows[0]) // tr,),
        in_specs=[pl.BlockSpec((n, tr, wp), lambda i: (0, first + i, 0))] + [pl.BlockSpec(memory_space=pl.ANY)] * len(held),
        out_specs=pl.BlockSpec((tr, W_IN_COLS), lambda i: (first + i, 0)),
        out_shape=jax.ShapeDtypeStruct((r, W_IN_COLS), g.dtype),
        input_output_aliases={1: 0} if held else {},
        compiler_params=_cp("parallel"),
    )(g, *held)


def _gate_cols(w):
    return jnp.pad(w, ((0, 0), (GOFF, GW - GOFF - D)))


def _gate_rows(w):
    return jnp.pad(w, ((GOFF, GW - GOFF - D), (0, 0)))


def _layout_wgk(w):
    r = w.shape[1]
    top = jnp.concatenate([w[0], jnp.zeros_like(w[0])], axis=1)
    bot = jnp.concatenate([jnp.zeros_like(w[1]), w[1]], axis=1)
    return jnp.concatenate([top, bot, jnp.zeros((DH - 2 * r, D), w.dtype)], axis=0)


def _unlayout_wgk(d, r=16):
    return jnp.stack([d[:r, :HW], d[r:2 * r, HW:]])


def _local_step(z, target, modc, modx, norms, onw, hg_lb, wgk, bgk, get_w_in, get_mix, get_ffn, send):
    n_pre1, n_post1, n_pre2, n_post2 = norms
    t = z[0].shape[0] + z[1].shape[0]
    tm = 1152 if t % 1152 == 0 else 256
    h1 = _prenorm(z, n_pre1, modc, modx, 0, 1, "prenorm1")
    w_in = get_w_in(h1)
    p = _matmul(h1, w_in, NN, BF16, "mm_in", t, 1024, D)
    q, v, k_f, k_b, g_f, g_b = _gates_fwd(p, hg_lb, wgk, bgk)
    o_f, st_f = _scan_fwd(q, k_f, v, g_f, False)
    o_b, st_b = _scan_fwd(q, k_b, v, g_b, True)
    y = _post_fwd(o_f, o_b, p, onw)
    w_br_hg, w_br_gla, w_out = get_mix(y)
    u1, u2, merged = _branch_merge(y, w_br_hg, w_br_gla, p)
    y1 = _matmul(merged, w_out, NN, BF16, "mm_out", tm, 512, GW)
    z1, h2 = _mid_fwd(z, y1, n_post1, n_pre2, modc, modx)
    w_gu_t, get_down = get_ffn(h2)
    u, v_ff, act = _mm_gu_act(h2, w_gu_t[0], w_gu_t[1], "mm_gu", tm)
    w_down = get_down(act)
    y2 =_matmul(act, w_down, NN, BF16, "mm_down", t, 512, D_FF)
    dz, dy2, loss_vec, sm_final = _final(z1, y2, target, n_post2, modc, modx)
    du, dv_ff = _mm_down_dx_act(dy2, w_down, u, v_ff, "mm_down_dx", tm)
    d_w_down = _matmul(act, dy2, TN, BF16, "mm_down_dw", D_FF // 2, 1024, t)
    dh2 = _matmul((du, dv_ff), w_gu_t, NN, BF16, "mm_gu_dx", tm, 512, D_FF)
    d_w_gate_t = _matmul(du, h2, TN, BF16, "mm_gate_dw", D_FF // 2, 1024, t)
    d_w_up_t = _matmul(dv_ff, h2, TN, BF16, "mm_up_dw", D_FF // 2, 1024, t)
    dh2 = send(("w_down", "w_gate_t", "w_up_t"), (d_w_down, d_w_gate_t, d_w_up_t), dh2)
    dz, dy1, sm_mid = _mid_bwd(dh2, dz, z1, y1, n_post1, n_pre2, modc, modx)
    dmerged = _matmul(dy1, w_out, NT, BF16, "mm_out_dx", tm, GW, D)
    d_w_out = _matmul(merged, dy1, TN, BF16, "mm_out_dw", GW, 512, t)
    du1, du2, dgm, dy_hg, dy_gla = _branch_merge_bwd(dmerged, p, u1, u2, w_br_hg, w_br_gla)
    d_w_br_hg = _matmul(y, du1, TN, BF16, "mm_br_hg_dw", HW, GW, t, a_off=0, m_out=HW)
    d_w_br_gla = _matmul(y, du2, TN, BF16, "mm_br_gla_dw", HW, GW, t, a_off=1, m_out=HW)
    dy_hg = send(("w_out", "w_br_hg", "w_br_gla"), (d_w_out, d_w_br_hg, d_w_br_gla), dy_hg)
    do, dgo, sm_post = _post_bwd(dy_hg, dy_gla, o_f, o_b, p, onw)
    dq_f, dk_f, dv_f, dg_f = _scan_bwd(q, k_f, v, g_f, do, st_f, False)
    dq_b, dk_b, dv_b, dg_b = _scan_bwd(q, k_b, v, g_b, do, st_b, True)
    dp, d_lb, d_wgk, d_bgk = _gates_bwd(p, hg_lb, wgk, bgk, dgm, dgo, dq_f, dq_b, dv_f, dv_b, dk_f, dk_b, dg_f, dg_b)
    d_w_in_a = _matmul(h1, dp, TN, BF16, "mm_in_dw_a", 512, 1024, t, a_off=0, m_out=D // 2)
    dp = send(("w_in_a",), (d_w_in_a,), dp)
    d_w_in_b = _matmul(h1, dp, TN, BF16, "mm_in_dw_b", 512, 1024, t, a_off=1, m_out=D // 2)
    dp = send(("w_in_b",), (d_w_in_b,), dp)
    dh1 = _matmul(dp, w_in, NT, BF16, "mm_in_dx", tm, 512, W_IN_COLS // 2)
    grad_x, sm_pre = _pre_bwd(dh1, dz, z, n_pre1, modc, modx)
    return dict(loss_vec=loss_vec, grad_x=grad_x, sm_final=sm_final, sm_mid=sm_mid, sm_post=sm_post, sm_pre=sm_pre,
                d_lb=d_lb, d_wgk=d_wgk, d_bgk=d_bgk)


MESH = pl.DeviceIdType.MESH
ANY = pl.BlockSpec(memory_space=pl.ANY)
N_REL = N_DEV - 1


def _place():
    return lax.axis_index("x"), lax.axis_index("y"), lax.axis_index("c")


def _slot(p):
    return 4 * p[0] + 2 * p[1] + p[2]


HBM = pl.BlockSpec(memory_space=pltpu.HBM)
SEM = pl.BlockSpec(memory_space=pltpu.SEMAPHORE)
EFFECT = pltpu.SideEffectType.DATAFLOW_SIDE_EFFECTING


def _peer_of(x, y, c, k):
    flip = lambda v, bit: 1 - v if bit else v
    return flip(x, k & 4), flip(y, k & 2), flip(c, k & 1)


def _view_whole(src, slot):
    return src


def _view_near(src, slot):
    return src


_view_near.peers = (1, 2, 4, 6)


def _view_near_rows(rows):
    def view(src, slot):
        return src.at[pl.ds(rows[0], rows[1] - rows[0])]
    view.peers = _view_near.peers
    view.land = lambda land, slot: land.at[slot, pl.ds(rows[0], rows[1] - rows[0])]
    return view


def _view_block(src, slot):
    return src.at[slot]


def _view_cols(src, slot):
    return src.at[:, pl.ds(pl.multiple_of(slot * (D // N_DEV), D // N_DEV), D // N_DEV)]


W_IN_SHARD = W_IN_REF // N_DEV


def _view_window(rows):
    def view(src, slot):
        col0 = pl.multiple_of((W_IN_SHARD * slot // DH) * DH, DH)
        return src.at[pl.ds(rows[0], rows[1] - rows[0]), pl.ds(col0, D)]
    return view


def _split_copies(view, srcs, lands, send_sems, recv_sems, local_sems):
    x, y, c = _place()
    me = _slot((x, y, c))
    into = getattr(view, "land", lambda land, slot: land.at[slot])
    local, sends, waits = [], [], []
    for a, (src, land) in enumerate(zip(srcs, lands)):
        local.append(pltpu.make_async_copy(view(src, me), into(land, me), local_sems.at[a]))
        for k in getattr(view, "peers", range(1, N_DEV)):
            peer = _peer_of(x, y, c, k)
            mine = view(src, _slot(peer))
            sems = dict(send_sem=send_sems.at[N_REL * a + k - 1], recv_sem=recv_sems.at[N_REL * a + k - 1],
                        device_id=peer, device_id_type=MESH)
            sends.append(pltpu.make_async_remote_copy(src_ref=mine, dst_ref=into(land, me), **sems))
            waits.append(pltpu.make_async_remote_copy(src_ref=mine, dst_ref=into(land, _slot(peer)), **sems))
    return local, sends, waits


def _split_start(groups, name, after):
    built = []
    for view, srcs, lands in groups:
        lands = [lax.empty(l, s.dtype) if isinstance(l, tuple) else l for l, s in zip(lands, srcs)]
        built.append((view, list(srcs), lands))
    bufs = [b for _, srcs, lands in built for b in srcs + lands]
    nb, ng = len(bufs), len(built)

    def body(*refs):
        buf_refs, sem_refs, token = refs[:nb], refs[nb + 1:nb + 1 + 3 * ng], refs[-1]
        pos = 0
        for i, (view, srcs, _) in enumerate(built):
            n = len(srcs)
            local, sends, _ = _split_copies(view, buf_refs[pos:pos + n], buf_refs[pos + n:pos + 2 * n],
                                            *sem_refs[3 * i:3 * i + 3])
            pos += 2 * n
            for cp in local + sends:
                cp.start()
        token[...] = jnp.zeros_like(token)

    sems = []
    for _, srcs, _ in built:
        n = len(srcs)
        sems += [pltpu.SemaphoreType.DMA((N_REL * n,)), pltpu.SemaphoreType.DMA((N_REL * n,)),
                 pltpu.SemaphoreType.DMA((n,))]
    hbm = lambda a: pltpu.with_memory_space_constraint(a, pltpu.HBM)
    out = pl.pallas_call(
        body, name=name,
        out_shape=(*sems, *[pltpu.HBM(b.shape, b.dtype) for b in bufs], jax.ShapeDtypeStruct((8, DH), F32)),
        in_specs=[HBM] * nb + [ANY],
        out_specs=(*([SEM] * (3 * ng)), *([HBM] * nb), pl.BlockSpec(memory_space=pltpu.VMEM)),
        input_output_aliases={i: 3 * ng + i for i in range(nb)},
        compiler_params=pltpu.CompilerParams(has_side_effects=EFFECT),
    )(*[hbm(b) for b in bufs], after)
    handles, pos = [], 3 * ng
    for i, (view, srcs, _) in enumerate(built):
        n = len(srcs)
        handles.append(dict(view=view, n=n, sems=out[3 * i:3 * i + 3], srcs=list(out[pos:pos + n]),
                            lands=list(out[pos + n:pos + 2 * n])))
        pos += 2 * n
    return handles, out[-1]


def _split_wait(handle, name, after, srcs=None, lands=None):
    view, n, sems = handle["view"], handle["n"], handle["sems"]
    srcs = handle["srcs"] if srcs is None else srcs
    lands = handle["lands"] if lands is None else lands
    afters = list(after) if isinstance(after, (list, tuple)) else [after]

    def body(*refs):
        src_refs, land_refs = refs[:n], refs[n:2 * n]
        send_sems, recv_sems, local_sems = refs[2 * n:2 * n + 3]
        local, _, waits = _split_copies(view, src_refs, land_refs, send_sems, recv_sems, local_sems)
        for cp in waits:
            cp.wait_send()
            cp.wait_recv()
        for cp in local:
            cp.wait()

    out = pl.pallas_call(
        body, name=name,
        out_shape=(*[pltpu.HBM(s.shape, s.dtype) for s in srcs], *[pltpu.HBM(l.shape, l.dtype) for l in lands]),
        in_specs=[HBM] * (2 * n) + [SEM, SEM, SEM] + [ANY] * len(afters),
        out_specs=tuple([HBM] * (2 * n)),
        input_output_aliases={i: i for i in range(2 * n)},
        compiler_params=pltpu.CompilerParams(has_side_effects=EFFECT),
    )(*srcs, *lands, *sems, *afters)
    handle["srcs"] = list(out[:n])
    return list(out[n:])


def _tie(x, token, name):
    def body(x_ref, t_ref, o_ref):
        pass

    return pl.pallas_call(
        body, name=name, out_shape=jax.ShapeDtypeStruct(x.shape, x.dtype),
        in_specs=[ANY, ANY], out_specs=ANY, input_output_aliases={0: 0},
    )(x, token)


def _forward_to_sibling(land, name, rows):
    def body(land_ref, out_ref, send_sems, recv_sems):
        x, y, c = _place()
        sibling = (x, y, 1 - c)
        chips = [(1 - x, y), (x, 1 - y), (1 - x, 1 - y)]
        piece = pl.ds(rows[0], rows[1] - rows[0])

        def copy(j, core):
            blk = _slot((*chips[j], core))
            return pltpu.make_async_remote_copy(src_ref=land_ref.at[blk, piece], dst_ref=out_ref.at[blk, piece],
                                                send_sem=send_sems.at[j], recv_sem=recv_sems.at[j],
                                                device_id=sibling, device_id_type=MESH)

        sends = [copy(j, c) for j in range(3)]
        for cp in sends:
            cp.start()
        for j in range(3):
            copy(j, 1 - c).wait_recv()
        for cp in sends:
            cp.wait_send()

    return pl.pallas_call(
        body, name=name, in_specs=[ANY], out_specs=ANY, input_output_aliases={0: 0},
        out_shape=jax.ShapeDtypeStruct(land.shape, land.dtype),
        scratch_shapes=[pltpu.SemaphoreType.DMA((3,)), pltpu.SemaphoreType.DMA((3,))],
    )(land)


def _mod_fwd(a, w, b):
    def body(a_ref, w_ref, b_ref, o_ref):
        o_ref[...] = _dot(_silu(a_ref[...]), w_ref[...], NN, precision=HI) + b_ref[...]

    return pl.pallas_call(
        body, name="mod_fwd", out_shape=jax.ShapeDtypeStruct((a.shape[0], w.shape[1]), F32),
        compiler_params=pltpu.CompilerParams(vmem_limit_bytes=VMEM_LIMIT),
    )(a, w, b)


def _mod_bwd(a, d, w):
    def body(a_ref, d_ref, w_ref, dw_ref, dc_ref):
        av = a_ref[...]
        dv = d_ref[...]
        dw_ref[...] = _dot(_silu(av), dv, TN, precision=HI)
        da = _dot(dv[0:8, :], w_ref[...], NT, precision=HI) * _dsilu(av[0:8, :])
        row = lax.broadcasted_iota(jnp.int32, da.shape, 0)
        dc_ref[...] = jnp.where(row == 0, da, 0.0)

    return pl.pallas_call(
        body, name="mod_bwd",
        out_shape=[jax.ShapeDtypeStruct(w.shape, F32), jax.ShapeDtypeStruct((8, w.shape[0]), F32)],
        compiler_params=pltpu.CompilerParams(vmem_limit_bytes=VMEM_LIMIT),
    )(a, d, w)


def _sum_devices(g):
    def body(g_ref, o_ref):
        acc = g_ref[0]
        for i in range(1, g.shape[0]):
            acc = acc + g_ref[i]
        o_ref[...] = acc

    return pl.pallas_call(body, name="sum_devices_%d" % g.shape[1],
                          out_shape=jax.ShapeDtypeStruct(g.shape[1:], F32))(g)


def _sum_windows(g, name):
    n, r, c = g.shape
    tr = 128

    def body(g_ref, o_ref):
        x, y, cc = _place()
        lane0 = (W_IN_SHARD * _slot((x, y, cc))) % DH
        acc = g_ref[0].astype(F32)
        for i in range(1, n):
            acc = acc + g_ref[i].astype(F32)
        o_ref[...] = pltpu.roll(acc, (c - lane0) % c, 1).T

    return pl.pallas_call(
        body, name=name, grid=(r // tr,),
        in_specs=[pl.BlockSpec((n, tr, c), lambda i: (0, i, 0))],
        out_specs=pl.BlockSpec((c, tr), lambda i: (0, i)),
        out_shape=jax.ShapeDtypeStruct((c, r), F32),
        compiler_params=_cp("parallel"),
    )(g)


def _adam_rows(r, c, n):
    budget = 10 * 1024 * 1024
    best = None
    for tr in range(16, r + 1, 16):
        if r % tr == 0 and tr * c * (2 * n + 28) <= budget:
            best = tr
    return best if best is not None else r


def _adamw(g, w, m, v, name):
    n, r, c = g.shape
    tr = _adam_rows(r, c, n)
    bc1 = 1.0 - ADAM_B1 ** ADAM_STEP
    bc2 = 1.0 - ADAM_B2 ** ADAM_STEP

    def body(g_ref, w_ref, m_ref, v_ref, go_ref, d_ref, mo_ref, vo_ref):
        grad = g_ref[0].astype(F32)
        for i in range(1, n):
            grad = grad + g_ref[i].astype(F32)
        go_ref[...] = grad
        m_new = ADAM_B1 * m_ref[...] + (1.0 - ADAM_B1) * grad
        v_new = ADAM_B2 * v_ref[...] + (1.0 - ADAM_B2) * (grad * grad)
        mo_ref[...] = m_new
        vo_ref[...] = v_new
        d_ref[...] = -ADAM_LR * ((m_new / bc1) / (jnp.sqrt(v_new / bc2) + ADAM_EPS) + ADAM_WD * w_ref[...])

    blk = pl.BlockSpec((tr, c), lambda i: (i, 0))
    out = jax.ShapeDtypeStruct((r, c), F32)
    return pl.pallas_call(
        body, name=name, grid=(r // tr,),
        in_specs=[pl.BlockSpec((n, tr, c), lambda i: (0, i, 0)), blk, blk, blk],
        out_specs=[blk] * 4, out_shape=[out] * 4,
        compiler_params=_cp("parallel"),
    )(g, w, m, v)


ADAM_ROWS3 = 168


def _adam_math(grad, w, m, v):
    bc1 = 1.0 - ADAM_B1 ** ADAM_STEP
    bc2 = 1.0 - ADAM_B2 ** ADAM_STEP
    m_new = ADAM_B1 * m + (1.0 - ADAM_B1) * grad
    v_new = ADAM_B2 * v + (1.0 - ADAM_B2) * (grad * grad)
    delta = -ADAM_LR * ((m_new / bc1) / (jnp.sqrt(v_new / bc2) + ADAM_EPS) + ADAM_WD * w)
    return delta, m_new, v_new


def _adamw_rows3(g, w3, m3, v3, name, cols, prev):
    r, _, _ = w3.shape
    c = cols[1] - cols[0]
    n = min(-(-r // 16) * 8, ADAM_ROWS3 * D // c // 8 * 8)
    starts = list(range(0, r - n, n)) + [r - n]
    held = [] if prev is None else list(prev)

    def body(g_hbm, w_hbm, m_hbm, v_hbm, *refs):
        go_hbm, d_hbm, mo_hbm, vo_hbm, gbuf, ibuf, obuf, in_sems, out_sems = refs[len(held):]
        part = lambda h, r0: h.at[pl.ds(r0, n), 0, pl.ds(cols[0], c)]

        def fetch(p):
            r0, slot = starts[p], p % 2
            g0 = (r0 // 8) * 8
            cps = [pltpu.make_async_copy(g_hbm.at[pl.ds(g0, n + 8)], gbuf.at[slot], in_sems.at[slot, 0])]
            cps += [pltpu.make_async_copy(part(h, r0), ibuf.at[slot, k], in_sems.at[slot, 1 + k])
                    for k, h in enumerate((w_hbm, m_hbm, v_hbm))]
            for cp in cps:
                cp.start()
            return cps

        pending, outs = fetch(0), []
        for p, r0 in enumerate(starts):
            slot = p % 2
            nxt = fetch(p + 1) if p + 1 < len(starts) else []
            for cp in pending:
                cp.wait()
            grad = gbuf[slot, pl.ds(r0 - (r0 // 8) * 8, n), :]
            delta, m_new, v_new = _adam_math(grad, ibuf[slot, 0], ibuf[slot, 1], ibuf[slot, 2])
            for cp in outs:
                cp.wait()
            for k, val in enumerate((grad, delta, m_new, v_new)):
                obuf[slot, k] = val
            outs = [pltpu.make_async_copy(obuf.at[slot, k], part(h, r0), out_sems.at[slot, k])
                    for k, h in enumerate((go_hbm, d_hbm, mo_hbm, vo_hbm))]
            for cp in outs:
                cp.start()
            pending = nxt
        for cp in outs:
            cp.wait()

    out = jax.ShapeDtypeStruct(w3.shape, F32)
    return pl.pallas_call(
        body, name=name, in_specs=[ANY] * (4 + len(held)), out_specs=[ANY] * 4, out_shape=[out] * 4,
        input_output_aliases={4 + k: k for k in range(len(held))},
        scratch_shapes=[pltpu.VMEM((2, n + 8, c), F32), pltpu.VMEM((2, 3, n, c), F32), pltpu.VMEM((2, 4, n, c), F32),
                        pltpu.SemaphoreType.DMA((2, 4)), pltpu.SemaphoreType.DMA((2, 4))],
        compiler_params=pltpu.CompilerParams(vmem_limit_bytes=VMEM_LIMIT),
    )(g, w3, m3, v3, *held)


def kernel(x, c, ctx, c_ctx, w_mod, b_mod, norm_pre1, norm_post1, norm_pre2, norm_post2, w_in, hg_lb, hg_onorm, gla_w_gk, gla_b_gk, gla_onorm, w_br_hg, w_br_gla, w_out, w_ff_gate, w_ff_up, w_ff_down, loss_target, m_c_ctx, m_w_mod, m_b_mod, m_norm_pre1, m_norm_post1, m_norm_pre2, m_norm_post2, m_w_in, m_hg_lb, m_hg_onorm, m_gla_w_gk, m_gla_b_gk, m_gla_onorm, m_w_br_hg, m_w_br_gla, m_w_out, m_w_ff_gate, m_w_ff_up, m_w_ff_down, v_c_ctx, v_w_mod, v_b_mod, v_norm_pre1, v_norm_post1, v_norm_pre2, v_norm_post2, v_w_in, v_hg_lb, v_hg_onorm, v_gla_w_gk, v_gla_b_gk, v_gla_onorm, v_w_br_hg, v_w_br_gla, v_w_out, v_w_ff_gate, v_w_ff_up, v_w_ff_down):
    xi, yi, ci = lax.axis_index("x"), lax.axis_index("y"), lax.axis_index("c")
    me = 4 * xi + 2 * yi + ci
    t = CTX + x.shape[1]

    w_in_pieces, w_in_state = [], {}

    def w_in_piece(i):
        return (_view_near_rows((i * W_IN_PIECE, (i + 1) * W_IN_PIECE)), w_in_state["src"], w_in_state["land"])

    def started_w_in(handle):
        w_in_state.update(src=handle["srcs"], land=handle["lands"])
        w_in_pieces.append(handle)

    tr_ = lambda a: jnp.swapaxes(a[0], 0, 1)
    w_in_bf = jnp.pad(w_in[0].astype(BF16), ((0, 0), (0, W_IN_PAD - W_IN_SHARD)))
    w_in_state.update(src=[w_in_bf], land=[lax.empty((N_DEV,) + w_in_bf.shape, BF16)])
    gathered = lambda arrs: [(N_DEV,) + a.shape for a in arrs]
    whole = lambda arrs: (_view_whole, arrs, gathered(arrs))
    small_in = [c, hg_lb, gla_w_gk[0], gla_b_gk[0]]
    (small_handle, piece), tok = _split_start([whole(small_in), w_in_piece(0)], "ag_small_start", c)
    started_w_in(piece)
    c_all, lb_g, wgk_g, bgk_g = _split_wait(small_handle, "ag_small_wait", tok)
    big = [w_in[0], w_br_hg[0], w_br_gla[0], w_out[0], tr_(w_ff_gate), tr_(w_ff_up), w_ff_down[0]]
    big_bf = [None] + [w.astype(BF16) for w in big[1:]]
    cols = lambda g: jnp.transpose(g, (1, 0, 2)).reshape(g.shape[1], N_DEV * g.shape[2])

    def get_w_in(after):
        w_full, first = None, 0
        for s, last in enumerate(W_IN_STAGES):
            for i in range(first, last):
                land = _split_wait(w_in_pieces[i], "ag_w_in_wait%d" % i, after if w_full is None else [after, w_full],
                                   srcs=w_in_state["src"], lands=w_in_state["land"])
                w_in_state.update(src=w_in_pieces[i]["srcs"], land=land)
            rows = (first * W_IN_PIECE, last * W_IN_PIECE)
            w_in_state["land"] = [_forward_to_sibling(w_in_state["land"][0], "ag_w_in_forward%d" % s, rows)]
            w_full = _assemble_w_in(w_in_state["land"][0], rows, w_full, "assemble_w_in%d" % s)
            first = last
        return w_full

    def get_mix(after):
        g_brh, g_brg, g_out = _split_wait(mix_handle, "ag_mix_wait", after)
        return _gate_cols(cols(g_brh)), _gate_cols(cols(g_brg)), _gate_rows(g_out.reshape(D, D))

    def get_ffn(after):
        g_gate, g_up = _split_wait(ffn_handle, "ag_ffn_wait", after)

        def get_down(after):
            g_down, = _split_wait(down_handle, "ag_down_wait", after)
            return g_down.reshape(D_FF, D)

        return (g_gate.reshape(D_FF, D), g_up.reshape(D_FF, D)), get_down

    hg_lb_full = jnp.transpose(lb_g, (1, 2, 0, 3)).reshape(2, 2, HW)
    wgk_k = _layout_wgk(jnp.transpose(wgk_g, (1, 2, 0, 3)).reshape(2, 16, HW)).astype(BF16)
    bgk_k = jnp.transpose(bgk_g, (1, 0, 2)).reshape(1, D)
    onw = jnp.concatenate([jnp.tile(hg_onorm, (1, NH // 2)), jnp.tile(gla_onorm, (1, NH // 2))], axis=1)

    n_mod = w_mod.shape[2]
    a9 = jnp.concatenate([c_ctx[None], c_all[:, 0], jnp.zeros((16 - 1 - N_DEV, D), F32)], axis=0)
    b_loc = lax.dynamic_slice(b_mod, (0, me * n_mod), (1, n_mod))
    s_loc = _mod_fwd(a9, w_mod[0], b_loc)
    (mod_handle, piece), tok = _split_start([whole([s_loc]), w_in_piece(1)], "ag_mod_start", s_loc)
    started_w_in(piece)
    for i in range(2, D // W_IN_PIECE):
        (piece,), tok = _split_start([w_in_piece(i)], "ag_w_in_start%d" % i, tok)
        started_w_in(piece)
    s_all, = _split_wait(mod_handle, "ag_mod_wait", tok)
    mod_all = jnp.transpose(s_all, (1, 0, 2)).reshape(16, N_DEV * n_mod)
    pad8 = lambda m: jnp.concatenate([m.reshape(6, D), jnp.zeros((2, D), F32)], axis=0)
    modc = pad8(mod_all[0])
    modx = pad8(lax.dynamic_slice(mod_all, (1 + me, 0), (1, N_DEV * n_mod))[0])

    (mix_handle, ffn_handle, down_handle), tok = _split_start(
        [whole(big_bf[1:4]), whole(big_bf[4:6]), whole(big_bf[6:])], "ag_big_start", s_all)

    z = (ctx[0], x[0])
    modx = _tie(modx, tok, "tie_mod")
    norms = (norm_pre1, norm_post1, norm_pre2, norm_post2)
    rowshard = lambda d: d.reshape(N_DEV, d.shape[0] // N_DEV, d.shape[1]).astype(BF16)
    sent, w_in_grad = [], {}

    def w_in_chunk(i):
        half, rows = W_IN_GRAD_CHUNKS[i]
        return (_view_window(rows), w_in_grad[half], [(N_DEV, rows[1] - rows[0], D)])

    def sent_w_in(i, handle):
        w_in_grad[W_IN_GRAD_CHUNKS[i][0]] = handle["srcs"]
        sent.append(("w_in%d" % i, ["w_in#%d" % i], handle))

    def send(names, grads, x_after):
        if names == ("w_in_a",):
            w_in_grad["a"] = list(grads)
            (handle,), tok = _split_start([w_in_chunk(0)], "grads_w_in0_start", x_after)
            sent_w_in(0, handle)
            return _tie(x_after, tok, "tie_w_in0")
        if names == ("w_in_b",):
            w_in_grad["b"] = list(grads)
            return x_after
        arrs, leaves, col_arrs, col_leaves = [], [], [], []
        for nm, g in zip(names, grads):
            if nm in ("w_gate_t", "w_up_t"):
                arrs.append(rowshard(g))
                leaves.append({"w_gate_t": "w_ff_gate", "w_up_t": "w_ff_up"}[nm])
            elif nm == "w_down":
                arrs.append(rowshard(g))
                leaves.append("w_ff_down")
            elif nm == "w_out":
                arrs.append(rowshard(g[GOFF:GOFF + D]))
                leaves.append(nm)
            else:
                col_arrs.append(g[:, GOFF:GOFF + D])
                col_leaves.append(nm)
        groups = [(_view_block, arrs, [a.shape for a in arrs])]
        if col_arrs:
            groups.append((_view_cols, col_arrs, [(N_DEV, a.shape[0], D // N_DEV) for a in col_arrs]))
        handles, tok = _split_start(groups, "grads_%s_start" % names[0], x_after)
        sent.append((names[0], leaves, handles[0]))
        if col_arrs:
            sent.append((names[0] + "_cols", col_leaves, handles[1]))
        return _tie(x_after, tok, "tie_" + names[0])

    r = _local_step(z, loss_target[0], modc, modx, norms, onw, hg_lb_full, wgk_k, bgk_k,
                    get_w_in, get_mix, get_ffn, send)
    grad_x = r["grad_x"][None]

    sm_pre, sm_mid, sm_fin = r["sm_pre"], r["sm_mid"], r["sm_final"]
    dmodc = jnp.stack([sm_pre[0], sm_pre[2], sm_mid[4], sm_mid[0], sm_mid[2], sm_fin[0]]).reshape(-1)
    dmodx = jnp.stack([sm_pre[1], sm_pre[3], sm_mid[5], sm_mid[1], sm_mid[3], sm_fin[1]]).reshape(-1)
    on = r["sm_post"][0].reshape(NH, DH)
    pieces = [dmodc, dmodx, sm_pre[4], sm_mid[7], sm_mid[6], sm_fin[2], on[:NH // 2].sum(0), on[NH // 2:].sum(0),
              r["d_lb"][:2].reshape(-1), _unlayout_wgk(r["d_wgk"]).reshape(-1), r["d_bgk"][0]]
    loss_local = (0.5 / D) * jnp.sum(r["loss_vec"])
    pieces.append(jnp.concatenate([loss_local.reshape(1), jnp.zeros((DH - 1,), F32)]))
    sizes = [p.shape[0] for p in pieces]
    pack = jnp.concatenate(pieces).reshape(-1, DH)
    moms = [(m_w_in, v_w_in), (m_w_br_hg, v_w_br_hg), (m_w_br_gla, v_w_br_gla), (m_w_out, v_w_out),
            (m_w_ff_gate, v_w_ff_gate), (m_w_ff_up, v_w_ff_up), (m_w_ff_down, v_w_ff_down)]
    names = ["w_in", "w_br_hg", "w_br_gla", "w_out", "w_ff_gate", "w_ff_up", "w_ff_down"]
    wmv = {nm: (w, m, v) for nm, w, (m, v) in zip(names, big, moms)}
    res = {}

    def update(nm):
        w, m, v = wmv[nm]
        if nm in ("w_ff_gate", "w_ff_up"):
            outs = _adamw(recv[nm], w, tr_(m), tr_(v), "adamw_" + nm)
            res[nm] = [jnp.swapaxes(o, 0, 1)[None] for o in outs]
        else:
            res[nm] = [o[None] for o in _adamw(recv[nm], w, m[0], v[0], "adamw_" + nm)]

    (small_handle, handle), tok = _split_start([whole([pack]), w_in_chunk(1)], "small_grads_start", pack)
    sent_w_in(1, handle)
    recv = {}
    for first, leaves, handle in sent:
        if not first.startswith("w_in"):
            recv.update(zip(leaves, _split_wait(handle, "grads_%s_wait" % first, tok)))
    update("w_ff_gate")
    update("w_ff_up")
    pack_all, = _split_wait(small_handle, "small_grads_wait", [res["w_ff_gate"][0], res["w_ff_up"][0]])
    tot = _sum_devices(pack_all).reshape(-1)
    offs = [sum(sizes[:i]) for i in range(len(sizes))]
    part = lambda i: tot[offs[i]:offs[i] + sizes[i]]
    dmodc_t, dmodx_t = part(0), part(1)
    g_b_mod = (dmodc_t + dmodx_t)[None]
    g_norms = [part(i)[None] for i in (2, 3, 4, 5)]
    g_hg_on, g_gla_on = part(6)[None], part(7)[None]
    lb0 = lax.dynamic_slice(part(8).reshape(2, HW), (0, me * (HW // N_DEV)), (2, HW // N_DEV))
    g_hg_lb = jnp.stack([lb0, -lb0])
    g_wgk = lax.dynamic_slice(part(9).reshape(2, 16, HW), (0, 0, me * (HW // N_DEV)), (2, 16, HW // N_DEV))[None]
    g_bgk = lax.dynamic_slice(part(10).reshape(2, HW), (0, me * (HW // N_DEV)), (2, HW // N_DEV))[None]
    loss = part(11)[0]

    dmx_all = pack_all.reshape(N_DEV, -1)[:, sizes[0]:sizes[0] + sizes[1]]
    d9 = jnp.concatenate([lax.dynamic_slice(dmodc_t[None], (0, me * n_mod), (1, n_mod)),
                          lax.dynamic_slice(dmx_all, (0, me * n_mod), (N_DEV, n_mod)),
                          jnp.zeros((16 - 1 - N_DEV, n_mod), F32)], axis=0)
    g_w_mod, dcc_part = _mod_bwd(a9, d9, w_mod[0])
    (cctx_handle, handle), tok = _split_start([whole([dcc_part]), w_in_chunk(2)], "c_ctx_start", dcc_part)
    sent_w_in(2, handle)
    recv["w_ff_down"] = _tie(recv["w_ff_down"], tok, "tie_down")
    update("w_ff_down")
    res["w_mod"] = [o[None] for o in _adamw(g_w_mod[None], w_mod[0], m_w_mod[0], v_w_mod[0], "adamw_w_mod")]
    for nm in ("w_out", "w_br_hg", "w_br_gla"):
        update(nm)
    dcc_all, = _split_wait(cctx_handle, "c_ctx_wait", [res["w_ff_down"][0], res["w_mod"][0]])
    g_c_ctx = _sum_devices(dcc_all)[0]

    small = [("c_ctx", c_ctx, m_c_ctx, v_c_ctx, g_c_ctx), ("b_mod", b_mod, m_b_mod, v_b_mod, g_b_mod),
             ("norm_pre1", norm_pre1, m_norm_pre1, v_norm_pre1, g_norms[0]),
             ("norm_post1", norm_post1, m_norm_post1, v_norm_post1, g_norms[1]),
             ("norm_pre2", norm_pre2, m_norm_pre2, v_norm_pre2, g_norms[2]),
             ("norm_post2", norm_post2, m_norm_post2, v_norm_post2, g_norms[3]),
             ("hg_lb", hg_lb, m_hg_lb, v_hg_lb, g_hg_lb), ("hg_onorm", hg_onorm, m_hg_onorm, v_hg_onorm, g_hg_on),
             ("gla_w_gk", gla_w_gk, m_gla_w_gk, v_gla_w_gk, g_wgk), ("gla_b_gk", gla_b_gk, m_gla_b_gk, v_gla_b_gk, g_bgk),
             ("gla_onorm", gla_onorm, m_gla_onorm, v_gla_onorm, g_gla_on)]
    flat = lambda k: jnp.concatenate([s[k].reshape(-1) for s in small]).reshape(-1, DH)
    outs = _adamw(flat(4)[None], flat(1), flat(2), flat(3), "adamw_small")
    off = 0
    for nm, w, _, _, _ in small:
        res[nm] = [o.reshape(-1)[off:off + w.size].reshape(w.shape) for o in outs]
        off += w.size

    done = [res[nm][0] for nm in names[1:]] + [res["w_mod"][0]] + [o for nm, *_ in small for o in res[nm]]
    major = lambda a: jnp.transpose(a, (2, 0, 1))
    outs, row0 = None, 0
    for i, (first, leaves, handle) in enumerate(s for s in sent if s[0].startswith("w_in")):
        half = W_IN_GRAD_CHUNKS[i][0]
        land, = _split_wait(handle, "grads_%s_wait" % first, done, srcs=w_in_grad[half])
        w_in_grad[half] = handle["srcs"]
        rows = (row0, row0 + land.shape[1])
        outs = _adamw_rows3(_sum_windows(land, "sum_windows%d" % i), major(w_in), major(m_w_in), major(v_w_in),
                            "adamw_w_in%d" % i, rows, outs)
        row0 = rows[1]
    res["w_in"] = [jnp.transpose(o, (1, 2, 0)) for o in outs]

    order = ["c_ctx", "w_mod", "b_mod", "norm_pre1", "norm_post1", "norm_pre2", "norm_post2", "w_in", "hg_lb",
             "hg_onorm", "gla_w_gk", "gla_b_gk", "gla_onorm", "w_br_hg", "w_br_gla", "w_out", "w_ff_gate", "w_ff_up",
             "w_ff_down"]
    return (loss, grad_x, *[res[n][k] for k in range(4) for n in order])
```

```python
import functools

import jax
import jax.numpy as jnp
from jax import lax
from jax.experimental import pallas as pl
from jax.experimental.pallas import tpu as pltpu

F32 = jnp.float32
BF16 = jnp.bfloat16
HI = lax.Precision.HIGHEST

N_DEV = 8
D = 1024
CTX = 256
HW = 512
DH = 128
NH = 8
D_FF = 2816
EPS = 1e-6
GLA_NORM = 16.0
CHUNK = 64
TR = 256
NCT = CTX // TR
W_IN_COLS = 7168
MAIN0 = 0
LR0 = 4608
GW = 1152
GOFF = 32
GATE_HG0 = LR0
GATE_GLA0 = LR0 + D
LEVELS = (32, 16, 8)
EXP_CLAMP = 80.0
VMEM_LIMIT = 48 * 1024 * 1024

ADAM_LR, ADAM_B1, ADAM_B2, ADAM_EPS, ADAM_WD, ADAM_STEP = 0.001, 0.9, 0.999, 1e-08, 0.01, 10


def _cp(*sem):
    return pltpu.CompilerParams(dimension_semantics=sem, vmem_limit_bytes=VMEM_LIMIT)


def _sig(x):
    return jax.nn.sigmoid(x)


def _silu(x):
    return x * _sig(x)


def _dsilu(x):
    s = _sig(x)
    return s * (1.0 + x * (1.0 - s))


def _rstd(x):
    return lax.rsqrt(jnp.mean(x * x, axis=-1, keepdims=True) + EPS)


def _rms_bwd(a, y, r):
    return r * (a - y * (r * r) * jnp.mean(a * y, axis=-1, keepdims=True))


def _colsum(x):
    return jnp.sum(x, axis=0, keepdims=True)


def _dot(a, b, dims, precision=None):
    return lax.dot_general(a, b, (dims, ((), ())), preferred_element_type=F32, precision=precision)


NN = ((1,), (0,))
NT = ((1,), (1,))
TN = ((0,), (0,))

SCAN_HEADS_FWD = 4
SCAN_HEADS_BWD = 4


def _split_dot(m, x):
    mb = m.astype(BF16)
    x1 = x.astype(BF16)
    r1 = x - x1.astype(F32)
    x2 = r1.astype(BF16)
    x3 = (r1 - x2.astype(F32)).astype(BF16)
    return _dot(mb, x1, NN) + _dot(mb, x2, NN) + _dot(mb, x3, NN)


def _matmul(a, b, dims, out_dtype, name, tm, tn, tk, a_off=0, m_out=None):
    a_pair = isinstance(a, (tuple, list))
    as_ = list(a) if a_pair else [a]
    a = as_[0]
    pair = isinstance(b, (tuple, list))
    bs = list(b) if pair else [b]
    b1 = bs[0]
    rows = b1.shape[0] * len(bs)
    half = None
    if dims == NN:
        m, k, n = a.shape[0], rows, b1.shape[1]
        a_spec = pl.BlockSpec((tm, tk), lambda i, j, kk: (i, kk + a_off))
        half = b1.shape[0] // tk
        if a_pair:
            assert pair and a.shape[1] == b1.shape[0] and a_off == 0
            a_spec = [pl.BlockSpec((tm, tk), lambda i, j, kk: (i, jnp.minimum(kk, half - 1))),
                      pl.BlockSpec((tm, tk), lambda i, j, kk: (i, jnp.maximum(kk - half, 0)))]
        b_maps = [lambda i, j, kk: (kk, j)] if not pair else [
            lambda i, j, kk: (jnp.minimum(kk, half - 1), j), lambda i, j, kk: (jnp.maximum(kk - half, 0), j)]
        b_specs = [pl.BlockSpec((tk, tn), f) for f in b_maps]
        axis = 2
    elif dims == NT:
        m, k, n = a.shape[0], b1.shape[1], rows
        a_spec = pl.BlockSpec((tm, tk), lambda i, j, kk: (i, kk + a_off))
        half = b1.shape[0] // tn
        b_maps = [lambda i, j, kk: (j, kk)] if not pair else [
            lambda i, j, kk: (jnp.minimum(j, half - 1), kk), lambda i, j, kk: (jnp.maximum(j - half, 0), kk)]
        b_specs = [pl.BlockSpec((tn, tk), f) for f in b_maps]
        axis = 1
    else:
        assert not pair
        m, k = (a.shape[1] if m_out is None else m_out), a.shape[0]
        n = b1.shape[1]
        a_spec = pl.BlockSpec((tk, tm), lambda i, j, kk: (kk, i + a_off))
        b_specs = [pl.BlockSpec((tk, tn), lambda i, j, kk: (kk, j))]
    assert m % tm == 0 and n % tn == 0 and k % tk == 0, (name, m, n, k, tm, tn, tk)
    nk = k // tk
    nb = len(bs)
    na = len(as_)
    assert na == 1 or dims == NN

    def body(*refs):
        a_refs, refs = refs[:na], refs[na:]
        o_ref = refs[nb]
        if pair:
            bv = jnp.where(pl.program_id(axis) < half, refs[0][...], refs[1][...])
        else:
            bv = refs[0][...]
        av = a_refs[0][...] if na == 1 else jnp.where(pl.program_id(2) < half, a_refs[0][...], a_refs[1][...])
        part = _dot(av, bv, dims)
        if nk == 1:
            o_ref[...] = part.astype(o_ref.dtype)
            return
        acc_ref = refs[nb + 1]
        kk = pl.program_id(2)

        @pl.when(kk == 0)
        def _():
            acc_ref[...] = part

        @pl.when(kk > 0)
        def _():
            acc_ref[...] += part

        @pl.when(kk == nk - 1)
        def _():
            o_ref[...] = acc_ref[...].astype(o_ref.dtype)

    return pl.pallas_call(
        body,
        name=name,
        grid=(m // tm, n // tn, nk),
        in_specs=(a_spec if a_pair else [a_spec]) + b_specs,
        out_specs=pl.BlockSpec((tm, tn), lambda i, j, kk: (i, j)),
        out_shape=jax.ShapeDtypeStruct((m, n), out_dtype),
        scratch_shapes=[] if nk == 1 else [pltpu.VMEM((tm, tn), F32)],
        compiler_params=_cp("parallel", "parallel", "arbitrary"),
    )(*as_, *bs)


def _mm_gu_act(h, w_gate_t, w_up_t, name, tm):
    t = h.shape[0]
    tn = D_FF // 2

    def body(a_ref, bg_ref, bu_ref, u_ref, v_ref, act_ref):
        a = a_ref[...]
        u = _dot(a, bg_ref[...], NT)
        v = _dot(a, bu_ref[...], NT)
        u_ref[...] = u.astype(BF16)
        v_ref[...] = v.astype(BF16)
        act_ref[...] = (_silu(u) * v).astype(BF16)

    wspec = pl.BlockSpec((tn, D), lambda i, j: (j, 0))
    ospec = pl.BlockSpec((tm, tn), lambda i, j: (i, j))
    out = jax.ShapeDtypeStruct((t, D_FF), BF16)
    return pl.pallas_call(
        body, name=name, grid=(t // tm, D_FF // tn),
        in_specs=[pl.BlockSpec((tm, D), lambda i, j: (i, 0)), wspec, wspec],
        out_specs=[ospec] * 3, out_shape=[out] * 3,
        compiler_params=_cp("parallel", "parallel"),
    )(h, w_gate_t, w_up_t)


def _mm_down_dx_act(dy, w_down, u, v, name, tm):
    t = dy.shape[0]
    tn = D_FF // 2

    def body(a_ref, b_ref, u_ref, v_ref, du_ref, dv_ref):
        dact = _dot(a_ref[...], b_ref[...], NT)
        u = u_ref[...].astype(F32)
        du_ref[...] = (dact * v_ref[...].astype(F32) * _dsilu(u)).astype(BF16)
        dv_ref[...] = (dact * _silu(u)).astype(BF16)

    ospec = pl.BlockSpec((tm, tn), lambda i, j: (i, j))
    out = jax.ShapeDtypeStruct((t, D_FF), BF16)
    return pl.pallas_call(
        body, name=name, grid=(t // tm, D_FF // tn),
        in_specs=[pl.BlockSpec((tm, D), lambda i, j: (i, 0)), pl.BlockSpec((tn, D), lambda i, j: (j, 0)), ospec, ospec],
        out_specs=[ospec] * 2, out_shape=[out] * 2,
        compiler_params=_cp("parallel", "parallel"),
    )(dy, w_down, u, v)


def _row(c):
    return pl.BlockSpec((TR, c), lambda i: (i, 0))


def _rowcol(width, cb):
    return pl.BlockSpec((TR, width), lambda i: (i, cb))


def _full(shape):
    return pl.BlockSpec(shape, lambda i: (0,) * len(shape))


def _mod_row(mc_ref, mx_ref, k, is_ctx):
    return jnp.where(is_ctx, mc_ref[k:k + 1, :], mx_ref[k:k + 1, :])


def _z_specs():
    return [pl.BlockSpec((TR, D), lambda i: (jnp.minimum(i, NCT - 1), 0)),
            pl.BlockSpec((TR, D), lambda i: (jnp.maximum(i - NCT, 0), 0))]


def _z_tile(c_ref, x_ref, is_ctx):
    return jnp.where(is_ctx, c_ref[...], x_ref[...])


def _acc_row(ref, k, val):
    ref[k:k + 1, :] += val


def _acc_mod(ref, k, is_ctx, val):
    zero = jnp.zeros_like(val)
    ref[k:k + 1, :] += jnp.where(is_ctx, val, zero)
    ref[k + 1:k + 2, :] += jnp.where(is_ctx, zero, val)


def _prenorm(z, nw, modc, modx, i_shift, i_scale, name):
    t = z[0].shape[0] + z[1].shape[0]

    def body(zc_ref, zx_ref, nw_ref, mc_ref, mx_ref, h_ref):
        is_ctx = pl.program_id(0) < NCT
        x = _z_tile(zc_ref, zx_ref, is_ctx)
        n = x * _rstd(x) * nw_ref[...]
        h = n * (1.0 + _mod_row(mc_ref, mx_ref, i_scale, is_ctx)) + _mod_row(mc_ref, mx_ref, i_shift, is_ctx)
        h_ref[...] = h.astype(BF16)

    return pl.pallas_call(
        body, name=name, grid=(t // TR,),
        in_specs=_z_specs() + [_full((1, D)), _full((8, D)), _full((8, D))],
        out_specs=_row(D),
        out_shape=jax.ShapeDtypeStruct((t, D), BF16),
        compiler_params=_cp("parallel"),
    )(*z, nw, modc, modx)


def _hg_lb(lb_ref, d):
    a0 = lb_ref[0, d:d + 1, :]
    a1 = lb_ref[1, d:d + 1, :]
    mx = jnp.maximum(a0, a1)
    e0 = jnp.exp(a0 - mx)
    e1 = jnp.exp(a1 - mx)
    return e0 / (e0 + e1)


def _log_sigmoid(x):
    return jnp.minimum(x, 0.0) - jnp.log(1.0 + jnp.exp(-jnp.abs(x)))


def _gates_fwd(p, hg_lb, wgk, bgk):
    t = p.shape[0]
    seg = lambda j: _rowcol(HW, MAIN0 // HW + j)

    def body(hq_ref, hi_ref, hf_ref, hb_ref, gq_ref, gk_ref, gv_ref, lr_ref, lb_ref, wgk_ref, bgk_ref,
             q_ref, v_ref, kf_ref, kb_ref, gf_ref, gb_ref):
        q_ref[:, :HW] = _silu(hq_ref[...].astype(F32)).astype(BF16)
        q_ref[:, HW:] = (gq_ref[...].astype(F32) * (DH ** -0.5)).astype(BF16)
        v_ref[:, :HW] = hi_ref[...]
        v_ref[:, HW:] = gv_ref[...]
        xg = _dot(lr_ref[...].astype(BF16), wgk_ref[...], NN) + bgk_ref[...]
        for d, (raw_ref, k_ref, g_ref) in enumerate(((hf_ref, kf_ref, gf_ref), (hb_ref, kb_ref, gb_ref))):
            lbd = _hg_lb(lb_ref, d)
            f = lbd + (1.0 - lbd) * _sig(raw_ref[...].astype(F32))
            k_ref[:, :HW] = (1.0 - f).astype(BF16)
            k_ref[:, HW:] = gk_ref[...]
            g_ref[:, :HW] = jnp.log(f)
            g_ref[:, HW:] = _log_sigmoid(xg[:, d * HW:(d + 1) * HW]) * (1.0 / GLA_NORM)

    out = jax.ShapeDtypeStruct((t, D), F32)
    outb = jax.ShapeDtypeStruct((t, D), BF16)
    return pl.pallas_call(
        body, name="gates_fwd", grid=(t // TR,),
        in_specs=[seg(0), seg(1), seg(2), seg(3), seg(5), seg(6), seg(7), _rowcol(DH, LR0 // DH),
                  _full((2, 2, HW)), _full((DH, D)), _full((1, D))],
        out_specs=[_row(D)] * 6,
        out_shape=[outb] * 4 + [out] * 2,
        compiler_params=_cp("parallel"),
    )(p, p, p, p, p, p, p, p, hg_lb, wgk, bgk)


def _post_fwd(o_fw, o_bw, p, onw):
    t = o_fw.shape[0]

    def body(of_ref, ob_ref, g1_ref, g2_ref, w_ref, y_ref):
        for h in range(NH):
            sl = slice(h * DH, (h + 1) * DH)
            o = of_ref[:, sl] + ob_ref[:, sl]
            g_ref = g1_ref if h < NH // 2 else g2_ref
            gs = slice((h % (NH // 2)) * DH, (h % (NH // 2) + 1) * DH)
            n = o * _rstd(o) * w_ref[:, sl]
            y_ref[:, sl] = (n * _silu(g_ref[:, gs].astype(F32))).astype(BF16)

    return pl.pallas_call(
        body, name="post_fwd", grid=(t // TR,),
        in_specs=[_row(D), _row(D), _rowcol(HW, MAIN0 // HW + 4), _rowcol(HW, MAIN0 // HW + 8), _full((1, D))],
        out_specs=_row(D),
        out_shape=jax.ShapeDtypeStruct((t, D), BF16),
        compiler_params=_cp("parallel"),
    )(o_fw, o_bw, p, p, onw)


def _gate_window_specs(col0):
    return [_rowcol(HW, col0 // HW), _rowcol(HW, col0 // HW + 1), _rowcol(DH, (col0 + 2 * HW) // DH)]


def _gate_window(refs):
    return jnp.concatenate([r[...].astype(F32) for r in refs], axis=1)


def _branch_merge(y, w_hg, w_gla, p):
    t = y.shape[0]

    def body(y_ref, wh_ref, wg_ref, a0, a1, a2, b0, b1, b2, u1_ref, u2_ref, m_ref):
        u1 = _dot(y_ref[:, :HW], wh_ref[...], NN)
        u2 = _dot(y_ref[:, HW:], wg_ref[...], NN)
        u1_ref[...] = u1.astype(BF16)
        u2_ref[...] = u2.astype(BF16)
        m_ref[...] = (_sig(_gate_window((a0, a1, a2))) * u1 + _sig(_gate_window((b0, b1, b2))) * u2).astype(BF16)

    out = jax.ShapeDtypeStruct((t, GW), BF16)
    return pl.pallas_call(
        body, name="branch_merge", grid=(t // TR,),
        in_specs=[_row(D), _full((HW, GW)), _full((HW, GW))] + _gate_window_specs(GATE_HG0)
        + _gate_window_specs(GATE_GLA0),
        out_specs=[_row(GW)] * 3, out_shape=[out] * 3,
        compiler_params=_cp("parallel"),
    )(y, w_hg, w_gla, p, p, p, p, p, p)


def _mid_fwd(z, y1, nw_post, nw_pre, modc, modx):
    t = y1.shape[0]

    def body(zc_ref, zx_ref, y_ref, wpo_ref, wpr_ref, mc_ref, mx_ref, z1_ref, h_ref):
        is_ctx = pl.program_id(0) < NCT
        y = y_ref[...].astype(F32)
        z1 = _z_tile(zc_ref, zx_ref, is_ctx) + _mod_row(mc_ref, mx_ref, 2, is_ctx) * (y * _rstd(y) * wpo_ref[...])
        z1_ref[...] = z1
        n = z1 * _rstd(z1) * wpr_ref[...]
        h = n * (1.0 + _mod_row(mc_ref, mx_ref, 4, is_ctx)) + _mod_row(mc_ref, mx_ref, 3, is_ctx)
        h_ref[...] = h.astype(BF16)

    return pl.pallas_call(
        body, name="mid_fwd", grid=(t // TR,),
        in_specs=_z_specs() + [_row(D), _full((1, D)), _full((1, D)), _full((8, D)), _full((8, D))],
        out_specs=[_row(D), _row(D)],
        out_shape=[jax.ShapeDtypeStruct((t, D), F32), jax.ShapeDtypeStruct((t, D), BF16)],
        compiler_params=_cp("parallel"),
    )(*z, y1, nw_post, nw_pre, modc, modx)


def _final(z1, y2, target, nw, modc, modx):
    t = z1.shape[0]

    def body(z1_ref, y_ref, tg_ref, w_ref, mc_ref, mx_ref, dz_ref, dy_ref, loss_ref, sm_ref):
        i = pl.program_id(0)
        is_ctx = i < NCT

        @pl.when(i == 0)
        def _():
            loss_ref[...] = jnp.zeros_like(loss_ref)
            sm_ref[...] = jnp.zeros_like(sm_ref)

        g = _mod_row(mc_ref, mx_ref, 5, is_ctx)
        y = y_ref[...].astype(F32)
        r = _rstd(y)
        w = w_ref[...]
        yr = y * r
        n = yr * w
        e = z1_ref[...] + g * n - tg_ref[...]
        lat = jnp.where(is_ctx, 0.0, 1.0)
        loss_ref[...] += lat * _colsum(e * e)
        dz = e * (lat / D)
        dz_ref[...] = dz
        _acc_mod(sm_ref, 0, is_ctx, _colsum(dz * n))
        dn = dz * g
        _acc_row(sm_ref, 2, _colsum(dn * yr))
        dy_ref[...] = _rms_bwd(dn * w, y, r).astype(BF16)

    return pl.pallas_call(
        body, name="final", grid=(t // TR,),
        in_specs=[_row(D), _row(D), pl.BlockSpec((TR, D), lambda i: (jnp.maximum(i - NCT, 0), 0)),
                  _full((1, D)), _full((8, D)), _full((8, D))],
        out_specs=[_row(D), _row(D), _full((1, D)), _full((8, D))],
        out_shape=[jax.ShapeDtypeStruct((t, D), F32), jax.ShapeDtypeStruct((t, D), BF16),
                   jax.ShapeDtypeStruct((1, D), F32), jax.ShapeDtypeStruct((8, D), F32)],
        compiler_params=_cp("arbitrary"),
    )(z1, y2, target, nw, modc, modx)


def _mid_bwd(dh2, dz, z1, y1, nw_post, nw_pre, modc, modx):
    t = z1.shape[0]

    def body(dh_ref, dz_ref, z1_ref, y_ref, wpo_ref, wpr_ref, mc_ref, mx_ref, dzo_ref, dy_ref, sm_ref):
        i = pl.program_id(0)
        is_ctx = i < NCT

        @pl.when(i == 0)
        def _():
            sm_ref[...] = jnp.zeros_like(sm_ref)

        dh = dh_ref[...].astype(F32)
        z1 = z1_ref[...]
        r = _rstd(z1)
        zr = z1 * r
        wpr = wpr_ref[...]
        n = zr * wpr
        _acc_mod(sm_ref, 0, is_ctx, _colsum(dh))
        _acc_mod(sm_ref, 2, is_ctx, _colsum(dh * n))
        dn = dh * (1.0 + _mod_row(mc_ref, mx_ref, 4, is_ctx))
        _acc_row(sm_ref, 6, _colsum(dn * zr))
        dz1 = dz_ref[...] + _rms_bwd(dn * wpr, z1, r)
        dzo_ref[...] = dz1
        y = y_ref[...].astype(F32)
        r1 = _rstd(y)
        yr = y * r1
        wpo = wpo_ref[...]
        g = _mod_row(mc_ref, mx_ref, 2, is_ctx)
        _acc_mod(sm_ref, 4, is_ctx, _colsum(dz1 * (yr * wpo)))
        dn1 = dz1 * g
        _acc_row(sm_ref, 7, _colsum(dn1 * yr))
        dy_ref[...] = _rms_bwd(dn1 * wpo, y, r1).astype(BF16)

    return pl.pallas_call(
        body, name="mid_bwd", grid=(t // TR,),
        in_specs=[_row(D)] * 4 + [_full((1, D)), _full((1, D)), _full((8, D)), _full((8, D))],
        out_specs=[_row(D), _row(D), _full((8, D))],
        out_shape=[jax.ShapeDtypeStruct((t, D), F32), jax.ShapeDtypeStruct((t, D), BF16),
                   jax.ShapeDtypeStruct((8, D), F32)],
        compiler_params=_cp("arbitrary"),
    )(dh2, dz, z1, y1, nw_post, nw_pre, modc, modx)


def _pre_bwd(dh1, dz, z, nw, modc, modx):
    t = dh1.shape[0]

    def body(dh_ref, dz_ref, zc_ref, zx_ref, w_ref, mc_ref, mx_ref, dzo_ref, sm_ref):
        i = pl.program_id(0)
        is_ctx = i < NCT

        @pl.when(i == 0)
        def _():
            sm_ref[...] = jnp.zeros_like(sm_ref)

        dh = dh_ref[...].astype(F32)
        x = _z_tile(zc_ref, zx_ref, is_ctx)
        r = _rstd(x)
        xr = x * r
        w = w_ref[...]
        _acc_mod(sm_ref, 0, is_ctx, _colsum(dh))
        _acc_mod(sm_ref, 2, is_ctx, _colsum(dh * (xr * w)))
        dn = dh * (1.0 + _mod_row(mc_ref, mx_ref, 1, is_ctx))
        _acc_row(sm_ref, 4, _colsum(dn * xr))
        dzo_ref[...] = dz_ref[...] + _rms_bwd(dn * w, x, r)

    return pl.pallas_call(
        body, name="pre_bwd", grid=(t // TR,),
        in_specs=[_row(D)] * 2 + _z_specs() + [_full((1, D)), _full((8, D)), _full((8, D))],
        out_specs=[pl.BlockSpec((TR, D), lambda i: (jnp.maximum(i - NCT, 0), 0)), _full((8, D))],
        out_shape=[jax.ShapeDtypeStruct((t - CTX, D), F32), jax.ShapeDtypeStruct((8, D), F32)],
        compiler_params=_cp("arbitrary"),
    )(dh1, dz, *z, nw, modc, modx)


def _branch_merge_bwd(dm, p, u1, u2, w_hg, w_gla):
    t = dm.shape[0]

    def body(dm_ref, a0, a1, a2, b0, b1, b2, u1_ref, u2_ref, wh_ref, wg_ref, du1_ref, du2_ref, dg_ref, dyh_ref, dyg_ref):
        dm_ = dm_ref[...].astype(F32)
        s1 = _sig(_gate_window((a0, a1, a2)))
        s2 = _sig(_gate_window((b0, b1, b2)))
        du1 = (dm_ * s1).astype(BF16)
        du2 = (dm_ * s2).astype(BF16)
        du1_ref[...] = du1
        du2_ref[...] = du2
        dg_ref[:, :GW] = (dm_ * u1_ref[...].astype(F32) * s1 * (1.0 - s1)).astype(BF16)
        dg_ref[:, GW:] = (dm_ * u2_ref[...].astype(F32) * s2 * (1.0 - s2)).astype(BF16)
        dyh_ref[...] = _dot(du1, wh_ref[...], NT).astype(BF16)
        dyg_ref[...] = _dot(du2, wg_ref[...], NT).astype(BF16)

    return pl.pallas_call(
        body, name="branch_merge_bwd", grid=(t // TR,),
        in_specs=[_row(GW)] + _gate_window_specs(GATE_HG0) + _gate_window_specs(GATE_GLA0)
        + [_row(GW), _row(GW), _full((HW, GW)), _full((HW, GW))],
        out_specs=[_row(GW), _row(GW), _row(2 * GW), _row(HW), _row(HW)],
        out_shape=[jax.ShapeDtypeStruct((t, GW), BF16), jax.ShapeDtypeStruct((t, GW), BF16),
                   jax.ShapeDtypeStruct((t, 2 * GW), BF16), jax.ShapeDtypeStruct((t, HW), BF16),
                   jax.ShapeDtypeStruct((t, HW), BF16)],
        compiler_params=_cp("parallel"),
    )(dm, p, p, p, p, p, p, u1, u2, w_hg, w_gla)


def _post_bwd(dy_hg, dy_gla, o_fw, o_bw, p, onw):
    t = o_fw.shape[0]

    def body(d1_ref, d2_ref, of_ref, ob_ref, g1_ref, g2_ref, w_ref, do_ref, dg_ref, sm_ref):
        @pl.when(pl.program_id(0) == 0)
        def _():
            sm_ref[...] = jnp.zeros_like(sm_ref)

        for h in range(NH):
            sl = slice(h * DH, (h + 1) * DH)
            gs = slice((h % (NH // 2)) * DH, (h % (NH // 2) + 1) * DH)
            g_ref, d_ref = (g1_ref, d1_ref) if h < NH // 2 else (g2_ref, d2_ref)
            o = of_ref[:, sl] + ob_ref[:, sl]
            r = _rstd(o)
            orr = o * r
            w = w_ref[:, sl]
            gt = g_ref[:, gs].astype(F32)
            dy = d_ref[:, gs].astype(F32)
            dg_ref[:, sl] = (dy * (orr * w) * _dsilu(gt)).astype(BF16)
            dn = dy * _silu(gt)
            sm_ref[0:1, sl] += _colsum(dn * orr)
            do_ref[:, sl] = _rms_bwd(dn * w, o, r)

    return pl.pallas_call(
        body, name="post_bwd", grid=(t // TR,),
        in_specs=[_row(HW), _row(HW), _row(D), _row(D), _rowcol(HW, MAIN0 // HW + 4), _rowcol(HW, MAIN0 // HW + 8),
                  _full((1, D))],
        out_specs=[_row(D), _row(D), _full((8, D))],
        out_shape=[jax.ShapeDtypeStruct((t, D), F32), jax.ShapeDtypeStruct((t, D), BF16),
                   jax.ShapeDtypeStruct((8, D), F32)],
        compiler_params=_cp("arbitrary"),
    )(dy_hg, dy_gla, o_fw, o_bw, p, p, onw)


def _gates_bwd(p, hg_lb, wgk, bgk, dgm, dgo, dq_f, dq_b, dv_f, dv_b, dk_f, dk_b, dg_f, dg_b):
    t = p.shape[0]
    seg = lambda j: _rowcol(HW, MAIN0 // HW + j)

    def body(hq_ref, hf_ref, hb_ref, lr_ref, lb_ref, wgk_ref, bgk_ref, dgm_ref, dgo_ref,
             dqf_ref, dqb_ref, dvf_ref, dvb_ref, dkf_ref, dkb_ref, dgf_ref, dgb_ref,
             dp_ref, dlb_ref, dw_ref, db_ref):
        @pl.when(pl.program_id(0) == 0)
        def _():
            dlb_ref[...] = jnp.zeros_like(dlb_ref)
            dw_ref[...] = jnp.zeros_like(dw_ref)
            db_ref[...] = jnp.zeros_like(db_ref)

        c0 = MAIN0

        def put(j, val):
            dp_ref[:, c0 + j * HW:c0 + (j + 1) * HW] = val.astype(BF16)

        dq = dqf_ref[...].astype(F32) + dqb_ref[...].astype(F32)
        dv = dvf_ref[...].astype(F32) + dvb_ref[...].astype(F32)
        put(0, dq[:, :HW] * _dsilu(hq_ref[...].astype(F32)))
        put(1, dv[:, :HW])
        put(5, dq[:, HW:] * (DH ** -0.5))
        put(7, dv[:, HW:])
        put(6, dkf_ref[:, HW:].astype(F32) + dkb_ref[:, HW:].astype(F32))
        dp_ref[:, c0 + 4 * HW:c0 + 5 * HW] = dgo_ref[:, :HW]
        dp_ref[:, c0 + 8 * HW:c0 + 9 * HW] = dgo_ref[:, HW:]
        lr = lr_ref[...].astype(BF16)
        xg = _dot(lr, wgk_ref[...], NN) + bgk_ref[...]
        dxg = []
        for d, (raw_ref, dk_ref, dg_ref) in enumerate(((hf_ref, dkf_ref, dgf_ref), (hb_ref, dkb_ref, dgb_ref))):
            lbd = _hg_lb(lb_ref, d)
            s = _sig(raw_ref[...].astype(F32))
            f = lbd + (1.0 - lbd) * s
            df = dg_ref[:, :HW] / f - dk_ref[:, :HW].astype(F32)
            put(2 + d, df * (1.0 - lbd) * s * (1.0 - s))
            dlb_ref[d:d + 1, :] += _colsum(df * (1.0 - s)) * (lbd * (1.0 - lbd))
            dxg.append(dg_ref[:, HW:] * (1.0 / GLA_NORM) * _sig(-xg[:, d * HW:(d + 1) * HW]))
        dxg = jnp.concatenate(dxg, axis=1)
        db_ref[0:1, :] += _colsum(dxg)
        dxg_b = dxg.astype(BF16)
        dw_ref[...] += _dot(lr, dxg_b, TN)
        dlr = _dot(dxg_b, wgk_ref[...], NT)
        dp_ref[:, LR0:LR0 + DH] = (dlr + dgm_ref[:, :DH].astype(F32)).astype(BF16)
        dp_ref[:, LR0 + DH:GATE_GLA0] = dgm_ref[:, DH:D]
        dp_ref[:, GATE_GLA0:GATE_GLA0 + DH] = dgm_ref[:, D:GW] + dgm_ref[:, GW:GW + DH]
        dp_ref[:, GATE_GLA0 + DH:GATE_GLA0 + GW] = dgm_ref[:, GW + DH:]
        dp_ref[:, GATE_GLA0 + GW:] = jnp.zeros((TR, W_IN_COLS - GATE_GLA0 - GW), BF16)

    return pl.pallas_call(
        body, name="gates_bwd", grid=(t // TR,),
        in_specs=[seg(0), seg(2), seg(3), _rowcol(DH, LR0 // DH), _full((2, 2, HW)), _full((DH, D)), _full((1, D)),
                  _row(2 * GW), _row(D)] + [_row(D)] * 8,
        out_specs=[_row(W_IN_COLS), _full((8, HW)), _full((DH, D)), _full((8, D))],
        out_shape=[jax.ShapeDtypeStruct((t, W_IN_COLS), BF16), jax.ShapeDtypeStruct((8, HW), F32),
                   jax.ShapeDtypeStruct((DH, D), F32), jax.ShapeDtypeStruct((8, D), F32)],
        compiler_params=_cp("arbitrary"),
    )(p, p, p, p, hg_lb, wgk, bgk, dgm, dgo, dq_f, dq_b, dv_f, dv_b, dk_f, dk_b, dg_f, dg_b)


def _scan_consts(rev):
    r = lax.broadcasted_iota(jnp.int32, (CHUNK, CHUNK), 0)
    u = lax.broadcasted_iota(jnp.int32, (CHUNK, CHUNK), 1)
    rp = lax.broadcasted_iota(jnp.int32, (CHUNK, 1), 0)
    if rev:
        r, u, rp = CHUNK - 1 - r, CHUNK - 1 - u, CHUNK - 1 - rp
    tri = jnp.where(u <= r, 1.0, 0.0).astype(F32)
    tri_t = jnp.where(r <= u, 1.0, 0.0).astype(F32)
    lv = []
    for b in LEVELS:
        sh = b.bit_length() - 1
        pair = ((r >> sh) == (u >> sh) + 1) & (((u >> sh) & 1) == 0)
        pair_t = ((u >> sh) == (r >> sh) + 1) & (((r >> sh) & 1) == 0)
        tside = ((rp >> sh) & 1) == 1
        lv.append((pair, pair_t, tside, jnp.where(tside, 1.0, -1.0).astype(F32)))
    bd = LEVELS[-1].bit_length() - 1
    diag = ((r >> bd) == (u >> bd)) & (u <= r)
    diag_t = ((r >> bd) == (u >> bd)) & (r <= u)
    return tri, tri_t, lv, diag, diag_t


def _row_of(pos, rev):
    return CHUNK - 1 - pos if rev else pos


def _chunk_terms(cum, b_scr, consts, rev):
    _, _, lv, _, _ = consts
    terms = []
    for b, (_, _, _, sgn) in zip(LEVELS, lv):
        pieces = []
        for j in range(CHUNK // (2 * b)):
            row = _row_of(2 * b * j + b - 1, rev)
            pieces.append(jnp.broadcast_to(b_scr[row:row + 1, :], (2 * b, DH)))
        if rev:
            pieces = pieces[::-1]
        bnd = pieces[0] if len(pieces) == 1 else jnp.concatenate(pieces, axis=0)
        terms.append(jnp.exp((cum - bnd) * sgn))
    b = LEVELS[-1]
    pieces = []
    for j in range(CHUNK // b):
        if j == 0:
            pieces.append(jnp.zeros((b, DH), F32))
        else:
            row = _row_of(b * j - 1, rev)
            pieces.append(jnp.broadcast_to(b_scr[row:row + 1, :], (b, DH)))
    if rev:
        pieces = pieces[::-1]
    start = jnp.concatenate(pieces, axis=0)
    wq = jnp.exp(jnp.minimum(cum - start, 0.0))
    wk = jnp.exp(jnp.minimum(start - cum, EXP_CLAMP))
    terms.append((wq, wk))
    return terms


def _run_staged(units):
    live = list(units)
    while live:
        nxt = []
        for u in live:
            try:
                next(u)
                nxt.append(u)
            except StopIteration:
                pass
        live = nxt


SCAN_TB = 256
SCAN_CB = SCAN_TB // CHUNK


def _block_order(i, ntb, rev):
    nctx = CTX // SCAN_TB
    if not rev:
        return i
    return jnp.where(i < nctx, nctx - 1 - i, ntb - 1 - (i - nctx))


def _chunk_in_block(j, rev):
    return SCAN_CB - 1 - j if rev else j


def _scan_fwd(q, k, v, g, rev):
    t = q.shape[0]
    nc = t // CHUNK
    hpb = SCAN_HEADS_FWD

    def body(q_ref, k_ref, v_ref, g_ref, o_ref, st_ref, s_scr, b_scr):
        consts = _scan_consts(rev)
        _, _, lv, diag, _ = consts
        masks = [lvl[0] for lvl in lv] + [diag]

        @pl.when(pl.program_id(1) == 0)
        def _():
            s_scr[...] = jnp.zeros_like(s_scr)

        tri = consts[0]
        state = {hh: s_scr[hh] for hh in range(hpb)}

        def unit(hh, j):
            sl = slice(hh * DH, (hh + 1) * DH)
            c = _chunk_in_block(j, rev)
            rows = slice(c * CHUNK, (c + 1) * CHUNK)
            b_ref = b_scr.at[hh * SCAN_CB + j]
            qc, kc, vc, gc = q_ref[rows, sl], k_ref[rows, sl], v_ref[rows, sl], g_ref[rows, sl]
            cum = _split_dot(tri, gc)
            b_ref[...] = cum
            yield
            terms = _chunk_terms(cum, b_ref, consts, rev)
            qf, kf = qc.astype(F32), kc.astype(F32)
            xs = [(jnp.where(tside, qf, kf) * w).astype(BF16) for w, (_, _, tside, _) in zip(terms[:-1], lv)]
            qd, kd = (qf * terms[-1][0]).astype(BF16), (kf * terms[-1][1]).astype(BF16)
            tot = _colsum(gc)
            qe = (qf * jnp.exp(cum)).astype(BF16)
            ke = (kf * jnp.exp(tot - cum)).astype(BF16)
            vb = vc.astype(BF16)
            yield
            scs = [_dot(x, x, NT) for x in xs] + [_dot(qd, kd, NT)]
            kv = _dot(vb, ke, TN)
            yield
            a = jnp.zeros((CHUNK, CHUNK), F32)
            for sc, m in zip(scs, masks):
                a = a + jnp.where(m, sc, 0.0)
            o_intra = _dot(a.astype(BF16), vb, NN)
            yield
            st = state[hh]
            st_ref[hh, c] = st
            o_ref[rows, sl] = o_intra + _dot(qe, st.astype(BF16), NT)
            state[hh] = st * jnp.exp(tot) + kv
            yield

        _run_staged([unit(hh, j) for hh in range(hpb) for j in range(SCAN_CB)])
        for hh in range(hpb):
            s_scr[hh] = state[hh]

    ntb = t // SCAN_TB
    col = pl.BlockSpec((SCAN_TB, hpb * DH), lambda h, i: (_block_order(i, ntb, rev), h))
    return pl.pallas_call(
        body, name="scan_fwd_" + ("bw" if rev else "fw"), grid=(NH // hpb, ntb),
        in_specs=[col] * 4,
        out_specs=[col, pl.BlockSpec((hpb, SCAN_CB, DH, DH), lambda h, i: (h, _block_order(i, ntb, rev), 0, 0))],
        out_shape=[jax.ShapeDtypeStruct((t, D), F32), jax.ShapeDtypeStruct((NH, nc, DH, DH), F32)],
        scratch_shapes=[pltpu.VMEM((hpb, DH, DH), F32), pltpu.VMEM((hpb * SCAN_CB, CHUNK, DH), F32)],
        compiler_params=_cp("parallel", "arbitrary"),
    )(q, k, v, g)


def _scan_bwd(q, k, v, g, do, states, rev):
    t = q.shape[0]
    nc = t // CHUNK
    hpb = SCAN_HEADS_BWD

    def body(q_ref, k_ref, v_ref, g_ref, do_ref, st_ref, dq_ref, dk_ref, dv_ref, dg_ref, ds_scr, b_scr):
        consts = _scan_consts(rev)
        _, tri_t, lv, diag, diag_t = consts
        masks = [(lvl[0], lvl[1]) for lvl in lv] + [(diag, diag_t)]
        @pl.when(pl.program_id(1) == 0)
        def _():
            ds_scr[...] = jnp.zeros_like(ds_scr)

        tri = consts[0]
        dstate = {hh: ds_scr[hh] for hh in range(hpb)}

        def unit(hh, jj):
            sl = slice(hh * DH, (hh + 1) * DH)
            c = _chunk_in_block(SCAN_CB - 1 - jj, rev)
            rows = slice(c * CHUNK, (c + 1) * CHUNK)
            b_ref = b_scr.at[hh * SCAN_CB + jj]
            qc, kc, vc, gc = q_ref[rows, sl], k_ref[rows, sl], v_ref[rows, sl], g_ref[rows, sl]
            dob = do_ref[rows, sl].astype(BF16)
            vb = vc.astype(BF16)
            cum = _split_dot(tri, gc)
            b_ref[...] = cum
            da = _dot(dob, vb, NT)
            da_t = _dot(vb, dob, NT)
            yield
            terms = _chunk_terms(cum, b_ref, consts, rev)
            qf, kf = qc.astype(F32), kc.astype(F32)
            xs = [(jnp.where(tside, qf, kf) * w).astype(BF16) for w, (_, _, tside, _) in zip(terms[:-1], lv)]
            wqd, wkd = terms[-1]
            qdb, kdb = (qf * wqd).astype(BF16), (kf * wkd).astype(BF16)
            tot = _colsum(gc)
            e_tot = jnp.exp(tot)
            e_b = jnp.exp(cum)
            e_t = jnp.exp(tot - cum)
            qeb = (qf * e_b).astype(BF16)
            keb = (kf * e_t).astype(BF16)
            dsym = [(jnp.where(m, da, 0.0) + jnp.where(m_t, da_t, 0.0)).astype(BF16) for m, m_t in masks[:-1]]
            dad = (jnp.where(diag, da, 0.0).astype(BF16), jnp.where(diag_t, da_t, 0.0).astype(BF16))
            yield
            sym = [_dot(x, x, NT) for x in xs]
            dxs = [_dot(d, x, NN) for d, x in zip(dsym, xs)]
            at_d = _dot(kdb, qdb, NT)
            dqt_d = _dot(dad[0], kdb, NN)
            dkt_d = _dot(dad[1], qdb, NN)
            qd = _dot(dob, qeb, TN)
            yield
            a_t = jnp.where(diag_t, at_d, 0.0)
            dq = dqt_d * wqd
            dk = dkt_d * wkd
            db = dqt_d * qdb.astype(F32) - dkt_d * kdb.astype(F32)
            for s, dx, x, w, (_, m_t, tside, sgn) in zip(sym, dxs, xs, terms[:-1], lv):
                a_t = a_t + jnp.where(m_t, s, 0.0)
                dxw = dx * w
                dq = dq + jnp.where(tside, dxw, 0.0)
                dk = dk + jnp.where(tside, 0.0, dxw)
                db = db + (dx * x.astype(F32)) * sgn
            dv_intra = _dot(a_t.astype(BF16), dob, NN)
            st = st_ref[hh, c]
            stb = st.astype(BF16)
            dqe = _dot(dob, stb, NN)
            yield
            dst = dstate[hh]
            dstb = dst.astype(BF16)
            dstate[hh] = dst * e_tot + qd
            dv_ref[rows, sl] = (dv_intra + _dot(keb, dstb, NT)).astype(BF16)
            dke = _dot(vb, dstb, NN)
            yield
            qe = qeb.astype(F32)
            ke = keb.astype(F32)
            dq_ref[rows, sl] = (dq + dqe * e_b).astype(BF16)
            dk_ref[rows, sl] = (dk + dke * e_t).astype(BF16)
            db = db + dqe * qe - dke * ke
            dtot = _colsum(dstb.astype(F32) * stb.astype(F32)) * e_tot + _colsum(dke * ke)
            dg_ref[rows, sl] = _split_dot(tri_t, db) + dtot
            yield

        _run_staged([unit(hh, jj) for hh in range(hpb) for jj in range(SCAN_CB)])
        for hh in range(hpb):
            ds_scr[hh] = dstate[hh]

    ntb = t // SCAN_TB
    blk = lambda i: _block_order(ntb - 1 - i, ntb, rev)
    col = pl.BlockSpec((SCAN_TB, hpb * DH), lambda h, i: (blk(i), h))
    out = jax.ShapeDtypeStruct((t, D), F32)
    outb = jax.ShapeDtypeStruct((t, D), BF16)
    return pl.pallas_call(
        body, name="scan_bwd_" + ("bw" if rev else "fw"), grid=(NH // hpb, ntb),
        in_specs=[col] * 5 + [pl.BlockSpec((hpb, SCAN_CB, DH, DH), lambda h, i: (h, blk(i), 0, 0))],
        out_specs=[col] * 4,
        out_shape=[outb] * 3 + [out],
        scratch_shapes=[pltpu.VMEM((hpb, DH, DH), F32), pltpu.VMEM((hpb * SCAN_CB, CHUNK, DH), F32)],
        compiler_params=_cp("parallel", "arbitrary"),
    )(q, k, v, g, do, states)


W_IN_GRAD_CHUNKS = (("a", (0, 512)), ("b", (0, 256)), ("b", (256, 512)))
W_IN_REF = 6688
W_IN_PAD = 896
W_IN_PIECE = 256
W_IN_STAGES = (3, 4)


def _assemble_w_in(g, rows, prev, name):
    n, r, wp = g.shape
    tr = W_IN_PIECE
    tiles = wp // DH
    first = rows[0] // tr

    def body(g_ref, *refs):
        o_ref = refs[-1]
        lane = lax.broadcasted_iota(jnp.int32, (tr, DH), 1)
        for t in range(W_IN_COLS // DH):
            acc = None
            for j in range(n):
                c = DH * t - W_IN_SHARD * j
                if c <= -DH or c >= W_IN_SHARD:
                    continue
                k, s = divmod(c, DH)
                lo = g_ref[j, :, k * DH:(k + 1) * DH] if 0 <= k < tiles else None
                hi = g_ref[j, :, (k + 1) * DH:(k + 2) * DH] if s and 0 <= k + 1 < tiles else None
                if s:
                    zero = jnp.zeros((tr, DH), g.dtype)
                    lo = zero if lo is None else pltpu.roll(lo, DH - s, 1)
                    hi = zero if hi is None else pltpu.roll(hi, DH - s, 1)
                    part = jnp.where(lane < DH - s, lo, hi)
                else:
                    part = lo
                acc = part if acc is None else acc + part
            o_ref[:, t * DH:(t + 1) * DH] = jnp.zeros((tr, DH), g.dtype) if acc is None else acc

    held = [] if prev is None else [prev]
    return pl.pallas_call(
        body, name=name, grid=((rows[1] - rows[0]) // tr,),
        in_specs=[pl.BlockSpec((n, tr, wp), lambda i: (0, first + i, 0))] + [pl.BlockSpec(memory_space=pl.ANY)] * len(held),
        out_specs=pl.BlockSpec((tr, W_IN_COLS), lambda i: (first + i, 0)),
        out_shape=jax.ShapeDtypeStruct((r, W_IN_COLS), g.dtype),
        input_output_aliases={1: 0} if held else {},
        compiler_params=_cp("parallel"),
    )(g, *held)


def _gate_cols(w):
    return jnp.pad(w, ((0, 0), (GOFF, GW - GOFF - D)))


def _gate_rows(w):
    return jnp.pad(w, ((GOFF, GW - GOFF - D), (0, 0)))


def _layout_wgk(w):
    r = w.shape[1]
    top = jnp.concatenate([w[0], jnp.zeros_like(w[0])], axis=1)
    bot = jnp.concatenate([jnp.zeros_like(w[1]), w[1]], axis=1)
    return jnp.concatenate([top, bot, jnp.zeros((DH - 2 * r, D), w.dtype)], axis=0)


def _unlayout_wgk(d, r=16):
    return jnp.stack([d[:r, :HW], d[r:2 * r, HW:]])


def _local_step(z, target, modc, modx, norms, onw, hg_lb, wgk, bgk, get_w_in, get_mix, get_ffn, send):
    n_pre1, n_post1, n_pre2, n_post2 = norms
    t = z[0].shape[0] + z[1].shape[0]
    tm = 1152 if t % 1152 == 0 else 256
    h1 = _prenorm(z, n_pre1, modc, modx, 0, 1, "prenorm1")
    w_in = get_w_in(h1)
    p = _matmul(h1, w_in, NN, BF16, "mm_in", t, 1024, D)
    q, v, k_f, k_b, g_f, g_b = _gates_fwd(p, hg_lb, wgk, bgk)
    o_f, st_f = _scan_fwd(q, k_f, v, g_f, False)
    o_b, st_b = _scan_fwd(q, k_b, v, g_b, True)
    y = _post_fwd(o_f, o_b, p, onw)
    w_br_hg, w_br_gla, w_out = get_mix(y)
    u1, u2, merged = _branch_merge(y, w_br_hg, w_br_gla, p)
    y1 = _matmul(merged, w_out, NN, BF16, "mm_out", tm, 512, GW)
    z1, h2 = _mid_fwd(z, y1, n_post1, n_pre2, modc, modx)
    w_gu_t, get_down = get_ffn(h2)
    u, v_ff, act = _mm_gu_act(h2, w_gu_t[0], w_gu_t[1], "mm_gu", tm)
    w_down = get_down(act)
    y2 =_matmul(act, w_down, NN, BF16, "mm_down", t, 512, D_FF)
    dz, dy2, loss_vec, sm_final = _final(z1, y2, target, n_post2, modc, modx)
    du, dv_ff = _mm_down_dx_act(dy2, w_down, u, v_ff, "mm_down_dx", tm)
    d_w_down = _matmul(act, dy2, TN, BF16, "mm_down_dw", D_FF // 2, 1024, t)
    dh2 = _matmul((du, dv_ff), w_gu_t, NN, BF16, "mm_gu_dx", tm, 512, D_FF)
    d_w_gate_t = _matmul(du, h2, TN, BF16, "mm_gate_dw", D_FF // 2, 1024, t)
    d_w_up_t = _matmul(dv_ff, h2, TN, BF16, "mm_up_dw", D_FF // 2, 1024, t)
    dh2 = send(("w_down", "w_gate_t", "w_up_t"), (d_w_down, d_w_gate_t, d_w_up_t), dh2)
    dz, dy1, sm_mid = _mid_bwd(dh2, dz, z1, y1, n_post1, n_pre2, modc, modx)
    dmerged = _matmul(dy1, w_out, NT, BF16, "mm_out_dx", tm, GW, D)
    d_w_out = _matmul(merged, dy1, TN, BF16, "mm_out_dw", GW, 512, t)
    du1, du2, dgm, dy_hg, dy_gla = _branch_merge_bwd(dmerged, p, u1, u2, w_br_hg, w_br_gla)
    d_w_br_hg = _matmul(y, du1, TN, BF16, "mm_br_hg_dw", HW, GW, t, a_off=0, m_out=HW)
    d_w_br_gla = _matmul(y, du2, TN, BF16, "mm_br_gla_dw", HW, GW, t, a_off=1, m_out=HW)
    dy_hg = send(("w_out", "w_br_hg", "w_br_gla"), (d_w_out, d_w_br_hg, d_w_br_gla), dy_hg)
    do, dgo, sm_post = _post_bwd(dy_hg, dy_gla, o_f, o_b, p, onw)
    dq_f, dk_f, dv_f, dg_f = _scan_bwd(q, k_f, v, g_f, do, st_f, False)
    dq_b, dk_b, dv_b, dg_b = _scan_bwd(q, k_b, v, g_b, do, st_b, True)
    dp, d_lb, d_wgk, d_bgk = _gates_bwd(p, hg_lb, wgk, bgk, dgm, dgo, dq_f, dq_b, dv_f, dv_b, dk_f, dk_b, dg_f, dg_b)
    d_w_in_a = _matmul(h1, dp, TN, BF16, "mm_in_dw_a", 512, 1024, t, a_off=0, m_out=D // 2)
    dp = send(("w_in_a",), (d_w_in_a,), dp)
    d_w_in_b = _matmul(h1, dp, TN, BF16, "mm_in_dw_b", 512, 1024, t, a_off=1, m_out=D // 2)
    dp = send(("w_in_b",), (d_w_in_b,), dp)
    dh1 = _matmul(dp, w_in, NT, BF16, "mm_in_dx", tm, 512, W_IN_COLS // 2)
    grad_x, sm_pre = _pre_bwd(dh1, dz, z, n_pre1, modc, modx)
    return dict(loss_vec=loss_vec, grad_x=grad_x, sm_final=sm_final, sm_mid=sm_mid, sm_post=sm_post, sm_pre=sm_pre,
                d_lb=d_lb, d_wgk=d_wgk, d_bgk=d_bgk)


MESH = pl.DeviceIdType.MESH
ANY = pl.BlockSpec(memory_space=pl.ANY)
N_REL = N_DEV - 1


def _place():
    return lax.axis_index("x"), lax.axis_index("y"), lax.axis_index("c")


def _slot(p):
    return 4 * p[0] + 2 * p[1] + p[2]


HBM = pl.BlockSpec(memory_space=pltpu.HBM)
SEM = pl.BlockSpec(memory_space=pltpu.SEMAPHORE)
EFFECT = pltpu.SideEffectType.DATAFLOW_SIDE_EFFECTING


def _peer_of(x, y, c, k):
    flip = lambda v, bit: 1 - v if bit else v
    return flip(x, k & 4), flip(y, k & 2), flip(c, k & 1)


def _view_whole(src, slot):
    return src


def _view_near(src, slot):
    return src


_view_near.peers = (1, 2, 4, 6)


def _view_near_rows(rows):
    def view(src, slot):
        return src.at[pl.ds(rows[0], rows[1] - rows[0])]
    view.peers = _view_near.peers
    view.land = lambda land, slot: land.at[slot, pl.ds(rows[0], rows[1] - rows[0])]
    return view


def _view_block(src, slot):
    return src.at[slot]


def _view_cols(src, slot):
    return src.at[:, pl.ds(pl.multiple_of(slot * (D // N_DEV), D // N_DEV), D // N_DEV)]


W_IN_SHARD = W_IN_REF // N_DEV


def _view_window(rows):
    def view(src, slot):
        col0 = pl.multiple_of((W_IN_SHARD * slot // DH) * DH, DH)
        return src.at[pl.ds(rows[0], rows[1] - rows[0]), pl.ds(col0, D)]
    return view


def _split_copies(view, srcs, lands, send_sems, recv_sems, local_sems):
    x, y, c = _place()
    me = _slot((x, y, c))
    into = getattr(view, "land", lambda land, slot: land.at[slot])
    local, sends, waits = [], [], []
    for a, (src, land) in enumerate(zip(srcs, lands)):
        local.append(pltpu.make_async_copy(view(src, me), into(land, me), local_sems.at[a]))
        for k in getattr(view, "peers", range(1, N_DEV)):
            peer = _peer_of(x, y, c, k)
            mine = view(src, _slot(peer))
            sems = dict(send_sem=send_sems.at[N_REL * a + k - 1], recv_sem=recv_sems.at[N_REL * a + k - 1],
                        device_id=peer, device_id_type=MESH)
            sends.append(pltpu.make_async_remote_copy(src_ref=mine, dst_ref=into(land, me), **sems))
            waits.append(pltpu.make_async_remote_copy(src_ref=mine, dst_ref=into(land, _slot(peer)), **sems))
    return local, sends, waits


def _split_start(groups, name, after):
    built = []
    for view, srcs, lands in groups:
        lands = [lax.empty(l, s.dtype) if isinstance(l, tuple) else l for l, s in zip(lands, srcs)]
        built.append((view, list(srcs), lands))
    bufs = [b for _, srcs, lands in built for b in srcs + lands]
    nb, ng = len(bufs), len(built)

    def body(*refs):
        buf_refs, sem_refs, token = refs[:nb], refs[nb + 1:nb + 1 + 3 * ng], refs[-1]
        pos = 0
        for i, (view, srcs, _) in enumerate(built):
            n = len(srcs)
            local, sends, _ = _split_copies(view, buf_refs[pos:pos + n], buf_refs[pos + n:pos + 2 * n],
                                            *sem_refs[3 * i:3 * i + 3])
            pos += 2 * n
            for cp in local + sends:
                cp.start()
        token[...] = jnp.zeros_like(token)

    sems = []
    for _, srcs, _ in built:
        n = len(srcs)
        sems += [pltpu.SemaphoreType.DMA((N_REL * n,)), pltpu.SemaphoreType.DMA((N_REL * n,)),
                 pltpu.SemaphoreType.DMA((n,))]
    hbm = lambda a: pltpu.with_memory_space_constraint(a, pltpu.HBM)
    out = pl.pallas_call(
        body, name=name,
        out_shape=(*sems, *[pltpu.HBM(b.shape, b.dtype) for b in bufs], jax.ShapeDtypeStruct((8, DH), F32)),
        in_specs=[HBM] * nb + [ANY],
        out_specs=(*([SEM] * (3 * ng)), *([HBM] * nb), pl.BlockSpec(memory_space=pltpu.VMEM)),
        input_output_aliases={i: 3 * ng + i for i in range(nb)},
        compiler_params=pltpu.CompilerParams(has_side_effects=EFFECT),
    )(*[hbm(b) for b in bufs], after)
    handles, pos = [], 3 * ng
    for i, (view, srcs, _) in enumerate(built):
        n = len(srcs)
        handles.append(dict(view=view, n=n, sems=out[3 * i:3 * i + 3], srcs=list(out[pos:pos + n]),
                            lands=list(out[pos + n:pos + 2 * n])))
        pos += 2 * n
    return handles, out[-1]


def _split_wait(handle, name, after, srcs=None, lands=None):
    view, n, sems = handle["view"], handle["n"], handle["sems"]
    srcs = handle["srcs"] if srcs is None else srcs
    lands = handle["lands"] if lands is None else lands
    afters = list(after) if isinstance(after, (list, tuple)) else [after]

    def body(*refs):
        src_refs, land_refs = refs[:n], refs[n:2 * n]
        send_sems, recv_sems, local_sems = refs[2 * n:2 * n + 3]
        local, _, waits = _split_copies(view, src_refs, land_refs, send_sems, recv_sems, local_sems)
        for cp in waits:
            cp.wait_send()
            cp.wait_recv()
        for cp in local:
            cp.wait()

    out = pl.pallas_call(
        body, name=name,
        out_shape=(*[pltpu.HBM(s.shape, s.dtype) for s in srcs], *[pltpu.HBM(l.shape, l.dtype) for l in lands]),
        in_specs=[HBM] * (2 * n) + [SEM, SEM, SEM] + [ANY] * len(afters),
        out_specs=tuple([HBM] * (2 * n)),
        input_output_aliases={i: i for i in range(2 * n)},
        compiler_params=pltpu.CompilerParams(has_side_effects=EFFECT),
    )(*srcs, *lands, *sems, *afters)
    handle["srcs"] = list(out[:n])
    return list(out[n:])


def _tie(x, token, name):
    def body(x_ref, t_ref, o_ref):
        pass

    return pl.pallas_call(
        body, name=name, out_shape=jax.ShapeDtypeStruct(x.shape, x.dtype),
        in_specs=[ANY, ANY], out_specs=ANY, input_output_aliases={0: 0},
    )(x, token)


def _forward_to_sibling(land, name, rows):
    def body(land_ref, out_ref, send_sems, recv_sems):
        x, y, c = _place()
        sibling = (x, y, 1 - c)
        chips = [(1 - x, y), (x, 1 - y), (1 - x, 1 - y)]
        piece = pl.ds(rows[0], rows[1] - rows[0])

        def copy(j, core):
            blk = _slot((*chips[j], core))
            return pltpu.make_async_remote_copy(src_ref=land_ref.at[blk, piece], dst_ref=out_ref.at[blk, piece],
                                                send_sem=send_sems.at[j], recv_sem=recv_sems.at[j],
                                                device_id=sibling, device_id_type=MESH)

        sends = [copy(j, c) for j in range(3)]
        for cp in sends:
            cp.start()
        for j in range(3):
            copy(j, 1 - c).wait_recv()
        for cp in sends:
            cp.wait_send()

    return pl.pallas_call(
        body, name=name, in_specs=[ANY], out_specs=ANY, input_output_aliases={0: 0},
        out_shape=jax.ShapeDtypeStruct(land.shape, land.dtype),
        scratch_shapes=[pltpu.SemaphoreType.DMA((3,)), pltpu.SemaphoreType.DMA((3,))],
    )(land)


def _mod_fwd(a, w, b):
    def body(a_ref, w_ref, b_ref, o_ref):
        o_ref[...] = _dot(_silu(a_ref[...]), w_ref[...], NN, precision=HI) + b_ref[...]

    return pl.pallas_call(
        body, name="mod_fwd", out_shape=jax.ShapeDtypeStruct((a.shape[0], w.shape[1]), F32),
        compiler_params=pltpu.CompilerParams(vmem_limit_bytes=VMEM_LIMIT),
    )(a, w, b)


def _mod_bwd(a, d, w):
    def body(a_ref, d_ref, w_ref, dw_ref, dc_ref):
        av = a_ref[...]
        dv = d_ref[...]
        dw_ref[...] = _dot(_silu(av), dv, TN, precision=HI)
        da = _dot(dv[0:8, :], w_ref[...], NT, precision=HI) * _dsilu(av[0:8, :])
        row = lax.broadcasted_iota(jnp.int32, da.shape, 0)
        dc_ref[...] = jnp.where(row == 0, da, 0.0)

    return pl.pallas_call(
        body, name="mod_bwd",
        out_shape=[jax.ShapeDtypeStruct(w.shape, F32), jax.ShapeDtypeStruct((8, w.shape[0]), F32)],
        compiler_params=pltpu.CompilerParams(vmem_limit_bytes=VMEM_LIMIT),
    )(a, d, w)


def _sum_devices(g):
    def body(g_ref, o_ref):
        acc = g_ref[0]
        for i in range(1, g.shape[0]):
            acc = acc + g_ref[i]
        o_ref[...] = acc

    return pl.pallas_call(body, name="sum_devices_%d" % g.shape[1],
                          out_shape=jax.ShapeDtypeStruct(g.shape[1:], F32))(g)


def _sum_windows(g, name):
    n, r, c = g.shape
    tr = 128

    def body(g_ref, o_ref):
        x, y, cc = _place()
        lane0 = (W_IN_SHARD * _slot((x, y, cc))) % DH
        acc = g_ref[0].astype(F32)
        for i in range(1, n):
            acc = acc + g_ref[i].astype(F32)
        o_ref[...] = pltpu.roll(acc, (c - lane0) % c, 1).T

    return pl.pallas_call(
        body, name=name, grid=(r // tr,),
        in_specs=[pl.BlockSpec((n, tr, c), lambda i: (0, i, 0))],
        out_specs=pl.BlockSpec((c, tr), lambda i: (0, i)),
        out_shape=jax.ShapeDtypeStruct((c, r), F32),
        compiler_params=_cp("parallel"),
    )(g)


def _adam_rows(r, c, n):
    budget = 10 * 1024 * 1024
    best = None
    for tr in range(16, r + 1, 16):
        if r % tr == 0 and tr * c * (2 * n + 28) <= budget:
            best = tr
    return best if best is not None else r


def _adamw(g, w, m, v, name):
    n, r, c = g.shape
    tr = _adam_rows(r, c, n)
    bc1 = 1.0 - ADAM_B1 ** ADAM_STEP
    bc2 = 1.0 - ADAM_B2 ** ADAM_STEP

    def body(g_ref, w_ref, m_ref, v_ref, go_ref, d_ref, mo_ref, vo_ref):
        grad = g_ref[0].astype(F32)
        for i in range(1, n):
            grad = grad + g_ref[i].astype(F32)
        go_ref[...] = grad
        m_new = ADAM_B1 * m_ref[...] + (1.0 - ADAM_B1) * grad
        v_new = ADAM_B2 * v_ref[...] + (1.0 - ADAM_B2) * (grad * grad)
        mo_ref[...] = m_new
        vo_ref[...] = v_new
        d_ref[...] = -ADAM_LR * ((m_new / bc1) / (jnp.sqrt(v_new / bc2) + ADAM_EPS) + ADAM_WD * w_ref[...])

    blk = pl.BlockSpec((tr, c), lambda i: (i, 0))
    out = jax.ShapeDtypeStruct((r, c), F32)
    return pl.pallas_call(
        body, name=name, grid=(r // tr,),
        in_specs=[pl.BlockSpec((n, tr, c), lambda i: (0, i, 0)), blk, blk, blk],
        out_specs=[blk] * 4, out_shape=[out] * 4,
        compiler_params=_cp("parallel"),
    )(g, w, m, v)


ADAM_ROWS3 = 168


def _adam_math(grad, w, m, v):
    bc1 = 1.0 - ADAM_B1 ** ADAM_STEP
    bc2 = 1.0 - ADAM_B2 ** ADAM_STEP
    m_new = ADAM_B1 * m + (1.0 - ADAM_B1) * grad
    v_new = ADAM_B2 * v + (1.0 - ADAM_B2) * (grad * grad)
    delta = -ADAM_LR * ((m_new / bc1) / (jnp.sqrt(v_new / bc2) + ADAM_EPS) + ADAM_WD * w)
    return delta, m_new, v_new


def _adamw_rows3(g, w3, m3, v3, name, cols, prev):
    r, _, _ = w3.shape
    c = cols[1] - cols[0]
    n = min(-(-r // 16) * 8, ADAM_ROWS3 * D // c // 8 * 8)
    starts = list(range(0, r - n, n)) + [r - n]
    held = [] if prev is None else list(prev)

    def body(g_hbm, w_hbm, m_hbm, v_hbm, *refs):
        go_hbm, d_hbm, mo_hbm, vo_hbm, gbuf, ibuf, obuf, in_sems, out_sems = refs[len(held):]
        part = lambda h, r0: h.at[pl.ds(r0, n), 0, pl.ds(cols[0], c)]

        def fetch(p):
            r0, slot = starts[p], p % 2
            g0 = (r0 // 8) * 8
            cps = [pltpu.make_async_copy(g_hbm.at[pl.ds(g0, n + 8)], gbuf.at[slot], in_sems.at[slot, 0])]
            cps += [pltpu.make_async_copy(part(h, r0), ibuf.at[slot, k], in_sems.at[slot, 1 + k])
                    for k, h in enumerate((w_hbm, m_hbm, v_hbm))]
            for cp in cps:
                cp.start()
            return cps

        pending, outs = fetch(0), []
        for p, r0 in enumerate(starts):
            slot = p % 2
            nxt = fetch(p + 1) if p + 1 < len(starts) else []
            for cp in pending:
                cp.wait()
            grad = gbuf[slot, pl.ds(r0 - (r0 // 8) * 8, n), :]
            delta, m_new, v_new = _adam_math(grad, ibuf[slot, 0], ibuf[slot, 1], ibuf[slot, 2])
            for cp in outs:
                cp.wait()
            for k, val in enumerate((grad, delta, m_new, v_new)):
                obuf[slot, k] = val
            outs = [pltpu.make_async_copy(obuf.at[slot, k], part(h, r0), out_sems.at[slot, k])
                    for k, h in enumerate((go_hbm, d_hbm, mo_hbm, vo_hbm))]
            for cp in outs:
                cp.start()
            pending = nxt
        for cp in outs:
            cp.wait()

    out = jax.ShapeDtypeStruct(w3.shape, F32)
    return pl.pallas_call(
        body, name=name, in_specs=[ANY] * (4 + len(held)), out_specs=[ANY] * 4, out_shape=[out] * 4,
        input_output_aliases={4 + k: k for k in range(len(held))},
        scratch_shapes=[pltpu.VMEM((2, n + 8, c), F32), pltpu.VMEM((2, 3, n, c), F32), pltpu.VMEM((2, 4, n, c), F32),
                        pltpu.SemaphoreType.DMA((2, 4)), pltpu.SemaphoreType.DMA((2, 4))],
        compiler_params=pltpu.CompilerParams(vmem_limit_bytes=VMEM_LIMIT),
    )(g, w3, m3, v3, *held)


def kernel(x, c, ctx, c_ctx, w_mod, b_mod, norm_pre1, norm_post1, norm_pre2, norm_post2, w_in, hg_lb, hg_onorm, gla_w_gk, gla_b_gk, gla_onorm, w_br_hg, w_br_gla, w_out, w_ff_gate, w_ff_up, w_ff_down, loss_target, m_c_ctx, m_w_mod, m_b_mod, m_norm_pre1, m_norm_post1, m_norm_pre2, m_norm_post2, m_w_in, m_hg_lb, m_hg_onorm, m_gla_w_gk, m_gla_b_gk, m_gla_onorm, m_w_br_hg, m_w_br_gla, m_w_out, m_w_ff_gate, m_w_ff_up, m_w_ff_down, v_c_ctx, v_w_mod, v_b_mod, v_norm_pre1, v_norm_post1, v_norm_pre2, v_norm_post2, v_w_in, v_hg_lb, v_hg_onorm, v_gla_w_gk, v_gla_b_gk, v_gla_onorm, v_w_br_hg, v_w_br_gla, v_w_out, v_w_ff_gate, v_w_ff_up, v_w_ff_down):
    xi, yi, ci = lax.axis_index("x"), lax.axis_index("y"), lax.axis_index("c")
    me = 4 * xi + 2 * yi + ci
    t = CTX + x.shape[1]

    w_in_pieces, w_in_state = [], {}

    def w_in_piece(i):
        return (_view_near_rows((i * W_IN_PIECE, (i + 1) * W_IN_PIECE)), w_in_state["src"], w_in_state["land"])

    def started_w_in(handle):
        w_in_state.update(src=handle["srcs"], land=handle["lands"])
        w_in_pieces.append(handle)

    tr_ = lambda a: jnp.swapaxes(a[0], 0, 1)
    w_in_bf = jnp.pad(w_in[0].astype(BF16), ((0, 0), (0, W_IN_PAD - W_IN_SHARD)))
    w_in_state.update(src=[w_in_bf], land=[lax.empty((N_DEV,) + w_in_bf.shape, BF16)])
    gathered = lambda arrs: [(N_DEV,) + a.shape for a in arrs]
    whole = lambda arrs: (_view_whole, arrs, gathered(arrs))
    small_in = [c, hg_lb, gla_w_gk[0], gla_b_gk[0]]
    (small_handle, piece), tok = _split_start([whole(small_in), w_in_piece(0)], "ag_small_start", c)
    started_w_in(piece)
    c_all, lb_g, wgk_g, bgk_g = _split_wait(small_handle, "ag_small_wait", tok)
    big = [w_in[0], w_br_hg[0], w_br_gla[0], w_out[0], tr_(w_ff_gate), tr_(w_ff_up), w_ff_down[0]]
    big_bf = [None] + [w.astype(BF16) for w in big[1:]]
    cols = lambda g: jnp.transpose(g, (1, 0, 2)).reshape(g.shape[1], N_DEV * g.shape[2])

    def get_w_in(after):
        w_full, first = None, 0
        for s, last in enumerate(W_IN_STAGES):
            for i in range(first, last):
                land = _split_wait(w_in_pieces[i], "ag_w_in_wait%d" % i, after if w_full is None else [after, w_full],
                                   srcs=w_in_state["src"], lands=w_in_state["land"])
                w_in_state.update(src=w_in_pieces[i]["srcs"], land=land)
            rows = (first * W_IN_PIECE, last * W_IN_PIECE)
            w_in_state["land"] = [_forward_to_sibling(w_in_state["land"][0], "ag_w_in_forward%d" % s, rows)]
            w_full = _assemble_w_in(w_in_state["land"][0], rows, w_full, "assemble_w_in%d" % s)
            first = last
        return w_full

    def get_mix(after):
        g_brh, g_brg, g_out = _split_wait(mix_handle, "ag_mix_wait", after)
        return _gate_cols(cols(g_brh)), _gate_cols(cols(g_brg)), _gate_rows(g_out.reshape(D, D))

    def get_ffn(after):
        g_gate, g_up = _split_wait(ffn_handle, "ag_ffn_wait", after)

        def get_down(after):
            g_down, = _split_wait(down_handle, "ag_down_wait", after)
            return g_down.reshape(D_FF, D)

        return (g_gate.reshape(D_FF, D), g_up.reshape(D_FF, D)), get_down

    hg_lb_full = jnp.transpose(lb_g, (1, 2, 0, 3)).reshape(2, 2, HW)
    wgk_k = _layout_wgk(jnp.transpose(wgk_g, (1, 2, 0, 3)).reshape(2, 16, HW)).astype(BF16)
    bgk_k = jnp.transpose(bgk_g, (1, 0, 2)).reshape(1, D)
    onw = jnp.concatenate([jnp.tile(hg_onorm, (1, NH // 2)), jnp.tile(gla_onorm, (1, NH // 2))], axis=1)

    n_mod = w_mod.shape[2]
    a9 = jnp.concatenate([c_ctx[None], c_all[:, 0], jnp.zeros((16 - 1 - N_DEV, D), F32)], axis=0)
    b_loc = lax.dynamic_slice(b_mod, (0, me * n_mod), (1, n_mod))
    s_loc = _mod_fwd(a9, w_mod[0], b_loc)
    (mod_handle, piece), tok = _split_start([whole([s_loc]), w_in_piece(1)], "ag_mod_start", s_loc)
    started_w_in(piece)
    for i in range(2, D // W_IN_PIECE):
        (piece,), tok = _split_start([w_in_piece(i)], "ag_w_in_start%d" % i, tok)
        started_w_in(piece)
    s_all, = _split_wait(mod_handle, "ag_mod_wait", tok)
    mod_all = jnp.transpose(s_all, (1, 0, 2)).reshape(16, N_DEV * n_mod)
    pad8 = lambda m: jnp.concatenate([m.reshape(6, D), jnp.zeros((2, D), F32)], axis=0)
    modc = pad8(mod_all[0])
    modx = pad8(lax.dynamic_slice(mod_all, (1 + me, 0), (1, N_DEV * n_mod))[0])

    (mix_handle, ffn_handle, down_handle), tok = _split_start(
        [whole(big_bf[1:4]), whole(big_bf[4:6]), whole(big_bf[6:])], "ag_big_start", s_all)

    z = (ctx[0], x[0])
    modx = _tie(modx, tok, "tie_mod")
    norms = (norm_pre1, norm_post1, norm_pre2, norm_post2)
    rowshard = lambda d: d.reshape(N_DEV, d.shape[0] // N_DEV, d.shape[1]).astype(BF16)
    sent, w_in_grad = [], {}

    def w_in_chunk(i):
        half, rows = W_IN_GRAD_CHUNKS[i]
        return (_view_window(rows), w_in_grad[half], [(N_DEV, rows[1] - rows[0], D)])

    def sent_w_in(i, handle):
        w_in_grad[W_IN_GRAD_CHUNKS[i][0]] = handle["srcs"]
        sent.append(("w_in%d" % i, ["w_in#%d" % i], handle))

    def send(names, grads, x_after):
        if names == ("w_in_a",):
            w_in_grad["a"] = list(grads)
            (handle,), tok = _split_start([w_in_chunk(0)], "grads_w_in0_start", x_after)
            sent_w_in(0, handle)
            return _tie(x_after, tok, "tie_w_in0")
        if names == ("w_in_b",):
            w_in_grad["b"] = list(grads)
            return x_after
        arrs, leaves, col_arrs, col_leaves = [], [], [], []
        for nm, g in zip(names, grads):
            if nm in ("w_gate_t", "w_up_t"):
                arrs.append(rowshard(g))
                leaves.append({"w_gate_t": "w_ff_gate", "w_up_t": "w_ff_up"}[nm])
            elif nm == "w_down":
                arrs.append(rowshard(g))
                leaves.append("w_ff_down")
            elif nm == "w_out":
                arrs.append(rowshard(g[GOFF:GOFF + D]))
                leaves.append(nm)
            else:
                col_arrs.append(g[:, GOFF:GOFF + D])
                col_leaves.append(nm)
        groups = [(_view_block, arrs, [a.shape for a in arrs])]
        if col_arrs:
            groups.append((_view_cols, col_arrs, [(N_DEV, a.shape[0], D // N_DEV) for a in col_arrs]))
        handles, tok = _split_start(groups, "grads_%s_start" % names[0], x_after)
        sent.append((names[0], leaves, handles[0]))
        if col_arrs:
            sent.append((names[0] + "_cols", col_leaves, handles[1]))
        return _tie(x_after, tok, "tie_" + names[0])

    r = _local_step(z, loss_target[0], modc, modx, norms, onw, hg_lb_full, wgk_k, bgk_k,
                    get_w_in, get_mix, get_ffn, send)
    grad_x = r["grad_x"][None]

    sm_pre, sm_mid, sm_fin = r["sm_pre"], r["sm_mid"], r["sm_final"]
    dmodc = jnp.stack([sm_pre[0], sm_pre[2], sm_mid[4], sm_mid[0], sm_mid[2], sm_fin[0]]).reshape(-1)
    dmodx = jnp.stack([sm_pre[1], sm_pre[3], sm_mid[5], sm_mid[1], sm_mid[3], sm_fin[1]]).reshape(-1)
    on = r["sm_post"][0].reshape(NH, DH)
    pieces = [dmodc, dmodx, sm_pre[4], sm_mid[7], sm_mid[6], sm_fin[2], on[:NH // 2].sum(0), on[NH // 2:].sum(0),
              r["d_lb"][:2].reshape(-1), _unlayout_wgk(r["d_wgk"]).reshape(-1), r["d_bgk"][0]]
    loss_local = (0.5 / D) * jnp.sum(r["loss_vec"])
    pieces.append(jnp.concatenate([loss_local.reshape(1), jnp.zeros((DH - 1,), F32)]))
    sizes = [p.shape[0] for p in pieces]
    pack = jnp.concatenate(pieces).reshape(-1, DH)
    moms = [(m_w_in, v_w_in), (m_w_br_hg, v_w_br_hg), (m_w_br_gla, v_w_br_gla), (m_w_out, v_w_out),
            (m_w_ff_gate, v_w_ff_gate), (m_w_ff_up, v_w_ff_up), (m_w_ff_down, v_w_ff_down)]
    names = ["w_in", "w_br_hg", "w_br_gla", "w_out", "w_ff_gate", "w_ff_up", "w_ff_down"]
    wmv = {nm: (w, m, v) for nm, w, (m, v) in zip(names, big, moms)}
    res, updated = {}, {}

    def update(nm):
        w, m, v = wmv[nm]
        if nm in ("w_ff_gate", "w_ff_up"):
            outs = _adamw(recv[nm], w, tr_(m), tr_(v), "adamw_" + nm)
            res[nm] = [jnp.swapaxes(o, 0, 1)[None] for o in outs]
        else:
            outs = _adamw(recv[nm], w, m[0], v[0], "adamw_" + nm)
            res[nm] = [o[None] for o in outs]
        updated[nm] = outs[0]

    (small_handle, handle), tok = _split_start([whole([pack]), w_in_chunk(1)], "small_grads_start", pack)
    sent_w_in(1, handle)
    recv = {}
    for first, leaves, handle in sent:
        if not first.startswith("w_in"):
            recv.update(zip(leaves, _split_wait(handle, "grads_%s_wait" % first, tok)))
    update("w_ff_gate")
    update("w_ff_up")
    pack_all, = _split_wait(small_handle, "small_grads_wait", [updated["w_ff_gate"], updated["w_ff_up"]])
    tot = _sum_devices(pack_all).reshape(-1)
    offs = [sum(sizes[:i]) for i in range(len(sizes))]
    part = lambda i: tot[offs[i]:offs[i] + sizes[i]]
    dmodc_t, dmodx_t = part(0), part(1)
    g_b_mod = (dmodc_t + dmodx_t)[None]
    g_norms = [part(i)[None] for i in (2, 3, 4, 5)]
    g_hg_on, g_gla_on = part(6)[None], part(7)[None]
    lb0 = lax.dynamic_slice(part(8).reshape(2, HW), (0, me * (HW // N_DEV)), (2, HW // N_DEV))
    g_hg_lb = jnp.stack([lb0, -lb0])
    g_wgk = lax.dynamic_slice(part(9).reshape(2, 16, HW), (0, 0, me * (HW // N_DEV)), (2, 16, HW // N_DEV))[None]
    g_bgk = lax.dynamic_slice(part(10).reshape(2, HW), (0, me * (HW // N_DEV)), (2, HW // N_DEV))[None]
    loss = part(11)[0]

    dmx_all = pack_all.reshape(N_DEV, -1)[:, sizes[0]:sizes[0] + sizes[1]]
    d9 = jnp.concatenate([lax.dynamic_slice(dmodc_t[None], (0, me * n_mod), (1, n_mod)),
                          lax.dynamic_slice(dmx_all, (0, me * n_mod), (N_DEV, n_mod)),
                          jnp.zeros((16 - 1 - N_DEV, n_mod), F32)], axis=0)
    g_w_mod, dcc_part = _mod_bwd(a9, d9, w_mod[0])
    (cctx_handle, handle), tok = _split_start([whole([dcc_part]), w_in_chunk(2)], "c_ctx_start", dcc_part)
    sent_w_in(2, handle)
    recv["w_ff_down"] = _tie(recv["w_ff_down"], tok, "tie_down")
    update("w_ff_down")
    res["w_mod"] = [o[None] for o in _adamw(g_w_mod[None], w_mod[0], m_w_mod[0], v_w_mod[0], "adamw_w_mod")]
    for nm in ("w_out", "w_br_hg", "w_br_gla"):
        update(nm)
    dcc_all, = _split_wait(cctx_handle, "c_ctx_wait", [updated["w_ff_down"], res["w_mod"][0]])
    g_c_ctx = _sum_devices(dcc_all)[0]

    small = [("c_ctx", c_ctx, m_c_ctx, v_c_ctx, g_c_ctx), ("b_mod", b_mod, m_b_mod, v_b_mod, g_b_mod),
             ("norm_pre1", norm_pre1, m_norm_pre1, v_norm_pre1, g_norms[0]),
             ("norm_post1", norm_post1, m_norm_post1, v_norm_post1, g_norms[1]),
             ("norm_pre2", norm_pre2, m_norm_pre2, v_norm_pre2, g_norms[2]),
             ("norm_post2", norm_post2, m_norm_post2, v_norm_post2, g_norms[3]),
             ("hg_lb", hg_lb, m_hg_lb, v_hg_lb, g_hg_lb), ("hg_onorm", hg_onorm, m_hg_onorm, v_hg_onorm, g_hg_on),
             ("gla_w_gk", gla_w_gk, m_gla_w_gk, v_gla_w_gk, g_wgk), ("gla_b_gk", gla_b_gk, m_gla_b_gk, v_gla_b_gk, g_bgk),
             ("gla_onorm", gla_onorm, m_gla_onorm, v_gla_onorm, g_gla_on)]
    flat = lambda k: jnp.concatenate([s[k].reshape(-1) for s in small]).reshape(-1, DH)
    outs = _adamw(flat(4)[None], flat(1), flat(2), flat(3), "adamw_small")
    off = 0
    for nm, w, _, _, _ in small:
        res[nm] = [o.reshape(-1)[off:off + w.size].reshape(w.shape) for o in outs]
        off += w.size

    done = [updated[nm] for nm in names[1:]] + [res["w_mod"][0]] + [o for nm, *_ in small for o in res[nm]]
    major = lambda a: jnp.transpose(a, (2, 0, 1))
    outs, row0 = None, 0
    for i, (first, leaves, handle) in enumerate(s for s in sent if s[0].startswith("w_in")):
        half = W_IN_GRAD_CHUNKS[i][0]
        land, = _split_wait(handle, "grads_%s_wait" % first, done, srcs=w_in_grad[half])
        w_in_grad[half] = handle["srcs"]
        rows = (row0, row0 + land.shape[1])
        outs = _adamw_rows3(_sum_windows(land, "sum_windows%d" % i), major(w_in), major(m_w_in), major(v_w_in),
                            "adamw_w_in%d" % i, rows, outs)
        row0 = rows[1]
    res["w_in"] = [jnp.transpose(o, (1, 2, 0)) for o in outs]

    order = ["c_ctx", "w_mod", "b_mod", "norm_pre1", "norm_post1", "norm_pre2", "norm_post2", "w_in", "hg_lb",
             "hg_onorm", "gla_w_gk", "gla_b_gk", "gla_onorm", "w_br_hg", "w_br_gla", "w_out", "w_ff_gate", "w_ff_up",
             "w_ff_down"]
    return (loss, grad_x, *[res[n][k] for k in range(4) for n in order])
```

```python
import functools

import jax
import jax.numpy as jnp
from jax import lax
from jax.experimental import pallas as pl
from jax.experimental.pallas import tpu as pltpu

F32 = jnp.float32
BF16 = jnp.bfloat16
HI = lax.Precision.HIGHEST

N_DEV = 8
D = 1024
CTX = 256
HW = 512
DH = 128
NH = 8
D_FF = 2816
EPS = 1e-6
GLA_NORM = 16.0
CHUNK = 64
TR = 256
NCT = CTX // TR
W_IN_COLS = 7168
MAIN0 = 0
LR0 = 4608
GW = 1152
GOFF = 32
GATE_HG0 = LR0
GATE_GLA0 = LR0 + D
LEVELS = (32, 16, 8)
EXP_CLAMP = 80.0
VMEM_LIMIT = 48 * 1024 * 1024

ADAM_LR, ADAM_B1, ADAM_B2, ADAM_EPS, ADAM_WD, ADAM_STEP = 0.001, 0.9, 0.999, 1e-08, 0.01, 10


def _cp(*sem):
    return pltpu.CompilerParams(dimension_semantics=sem, vmem_limit_bytes=VMEM_LIMIT)


def _sig(x):
    return jax.nn.sigmoid(x)


def _silu(x):
    return x * _sig(x)


def _dsilu(x):
    s = _sig(x)
    return s * (1.0 + x * (1.0 - s))


def _rstd(x):
    return lax.rsqrt(jnp.mean(x * x, axis=-1, keepdims=True) + EPS)


def _rms_bwd(a, y, r):
    return r * (a - y * (r * r) * jnp.mean(a * y, axis=-1, keepdims=True))


def _colsum(x):
    return jnp.sum(x, axis=0, keepdims=True)


def _dot(a, b, dims, precision=None):
    return lax.dot_general(a, b, (dims, ((), ())), preferred_element_type=F32, precision=precision)


NN = ((1,), (0,))
NT = ((1,), (1,))
TN = ((0,), (0,))

SCAN_HEADS_FWD = 4
SCAN_HEADS_BWD = 4


def _split_dot(m, x):
    mb = m.astype(BF16)
    x1 = x.astype(BF16)
    r1 = x - x1.astype(F32)
    x2 = r1.astype(BF16)
    x3 = (r1 - x2.astype(F32)).astype(BF16)
    return _dot(mb, x1, NN) + _dot(mb, x2, NN) + _dot(mb, x3, NN)


def _matmul(a, b, dims, out_dtype, name, tm, tn, tk, a_off=0, m_out=None):
    a_pair = isinstance(a, (tuple, list))
    as_ = list(a) if a_pair else [a]
    a = as_[0]
    pair = isinstance(b, (tuple, list))
    bs = list(b) if pair else [b]
    b1 = bs[0]
    rows = b1.shape[0] * len(bs)
    half = None
    if dims == NN:
        m, k, n = a.shape[0], rows, b1.shape[1]
        a_spec = pl.BlockSpec((tm, tk), lambda i, j, kk: (i, kk + a_off))
        half = b1.shape[0] // tk
        if a_pair:
            assert pair and a.shape[1] == b1.shape[0] and a_off == 0
            a_spec = [pl.BlockSpec((tm, tk), lambda i, j, kk: (i, jnp.minimum(kk, half - 1))),
                      pl.BlockSpec((tm, tk), lambda i, j, kk: (i, jnp.maximum(kk - half, 0)))]
        b_maps = [lambda i, j, kk: (kk, j)] if not pair else [
            lambda i, j, kk: (jnp.minimum(kk, half - 1), j), lambda i, j, kk: (jnp.maximum(kk - half, 0), j)]
        b_specs = [pl.BlockSpec((tk, tn), f) for f in b_maps]
        axis = 2
    elif dims == NT:
        m, k, n = a.shape[0], b1.shape[1], rows
        a_spec = pl.BlockSpec((tm, tk), lambda i, j, kk: (i, kk + a_off))
        half = b1.shape[0] // tn
        b_maps = [lambda i, j, kk: (j, kk)] if not pair else [
            lambda i, j, kk: (jnp.minimum(j, half - 1), kk), lambda i, j, kk: (jnp.maximum(j - half, 0), kk)]
        b_specs = [pl.BlockSpec((tn, tk), f) for f in b_maps]
        axis = 1
    else:
        assert not pair
        m, k = (a.shape[1] if m_out is None else m_out), a.shape[0]
        n = b1.shape[1]
        a_spec = pl.BlockSpec((tk, tm), lambda i, j, kk: (kk, i + a_off))
        b_specs = [pl.BlockSpec((tk, tn), lambda i, j, kk: (kk, j))]
    assert m % tm == 0 and n % tn == 0 and k % tk == 0, (name, m, n, k, tm, tn, tk)
    nk = k // tk
    nb = len(bs)
    na = len(as_)
    assert na == 1 or dims == NN

    def body(*refs):
        a_refs, refs = refs[:na], refs[na:]
        o_ref = refs[nb]
        if pair:
            bv = jnp.where(pl.program_id(axis) < half, refs[0][...], refs[1][...])
        else:
            bv = refs[0][...]
        av = a_refs[0][...] if na == 1 else jnp.where(pl.program_id(2) < half, a_refs[0][...], a_refs[1][...])
        part = _dot(av, bv, dims)
        if nk == 1:
            o_ref[...] = part.astype(o_ref.dtype)
            return
        acc_ref = refs[nb + 1]
        kk = pl.program_id(2)

        @pl.when(kk == 0)
        def _():
            acc_ref[...] = part

        @pl.when(kk > 0)
        def _():
            acc_ref[...] += part

        @pl.when(kk == nk - 1)
        def _():
            o_ref[...] = acc_ref[...].astype(o_ref.dtype)

    return pl.pallas_call(
        body,
        name=name,
        grid=(m // tm, n // tn, nk),
        in_specs=(a_spec if a_pair else [a_spec]) + b_specs,
        out_specs=pl.BlockSpec((tm, tn), lambda i, j, kk: (i, j)),
        out_shape=jax.ShapeDtypeStruct((m, n), out_dtype),
        scratch_shapes=[] if nk == 1 else [pltpu.VMEM((tm, tn), F32)],
        compiler_params=_cp("parallel", "parallel", "arbitrary"),
    )(*as_, *bs)


def _mm_gu_act(h, w_gate_t, w_up_t, name, tm):
    t = h.shape[0]
    tn = D_FF // 2

    def body(a_ref, bg_ref, bu_ref, u_ref, v_ref, act_ref):
        a = a_ref[...]
        u = _dot(a, bg_ref[...], NT)
        v = _dot(a, bu_ref[...], NT)
        u_ref[...] = u.astype(BF16)
        v_ref[...] = v.astype(BF16)
        act_ref[...] = (_silu(u) * v).astype(BF16)

    wspec = pl.BlockSpec((tn, D), lambda i, j: (j, 0))
    ospec = pl.BlockSpec((tm, tn), lambda i, j: (i, j))
    out = jax.ShapeDtypeStruct((t, D_FF), BF16)
    return pl.pallas_call(
        body, name=name, grid=(t // tm, D_FF // tn),
        in_specs=[pl.BlockSpec((tm, D), lambda i, j: (i, 0)), wspec, wspec],
        out_specs=[ospec] * 3, out_shape=[out] * 3,
        compiler_params=_cp("parallel", "parallel"),
    )(h, w_gate_t, w_up_t)


def _mm_down_dx_act(dy, w_down, u, v, name, tm):
    t = dy.shape[0]
    tn = D_FF // 2

    def body(a_ref, b_ref, u_ref, v_ref, du_ref, dv_ref):
        dact = _dot(a_ref[...], b_ref[...], NT)
        u = u_ref[...].astype(F32)
        du_ref[...] = (dact * v_ref[...].astype(F32) * _dsilu(u)).astype(BF16)
        dv_ref[...] = (dact * _silu(u)).astype(BF16)

    ospec = pl.BlockSpec((tm, tn), lambda i, j: (i, j))
    out = jax.ShapeDtypeStruct((t, D_FF), BF16)
    return pl.pallas_call(
        body, name=name, grid=(t // tm, D_FF // tn),
        in_specs=[pl.BlockSpec((tm, D), lambda i, j: (i, 0)), pl.BlockSpec((tn, D), lambda i, j: (j, 0)), ospec, ospec],
        out_specs=[ospec] * 2, out_shape=[out] * 2,
        compiler_params=_cp("parallel", "parallel"),
    )(dy, w_down, u, v)


def _row(c):
    return pl.BlockSpec((TR, c), lambda i: (i, 0))


def _rowcol(width, cb):
    return pl.BlockSpec((TR, width), lambda i: (i, cb))


def _full(shape):
    return pl.BlockSpec(shape, lambda i: (0,) * len(shape))


def _mod_row(mc_ref, mx_ref, k, is_ctx):
    return jnp.where(is_ctx, mc_ref[k:k + 1, :], mx_ref[k:k + 1, :])


def _z_specs():
    return [pl.BlockSpec((TR, D), lambda i: (jnp.minimum(i, NCT - 1), 0)),
            pl.BlockSpec((TR, D), lambda i: (jnp.maximum(i - NCT, 0), 0))]


def _z_tile(c_ref, x_ref, is_ctx):
    return jnp.where(is_ctx, c_ref[...], x_ref[...])


def _acc_row(ref, k, val):
    ref[k:k + 1, :] += val


def _acc_mod(ref, k, is_ctx, val):
    zero = jnp.zeros_like(val)
    ref[k:k + 1, :] += jnp.where(is_ctx, val, zero)
    ref[k + 1:k + 2, :] += jnp.where(is_ctx, zero, val)


def _prenorm(z, nw, modc, modx, i_shift, i_scale, name):
    t = z[0].shape[0] + z[1].shape[0]

    def body(zc_ref, zx_ref, nw_ref, mc_ref, mx_ref, h_ref):
        is_ctx = pl.program_id(0) < NCT
        x = _z_tile(zc_ref, zx_ref, is_ctx)
        n = x * _rstd(x) * nw_ref[...]
        h = n * (1.0 + _mod_row(mc_ref, mx_ref, i_scale, is_ctx)) + _mod_row(mc_ref, mx_ref, i_shift, is_ctx)
        h_ref[...] = h.astype(BF16)

    return pl.pallas_call(
        body, name=name, grid=(t // TR,),
        in_specs=_z_specs() + [_full((1, D)), _full((8, D)), _full((8, D))],
        out_specs=_row(D),
        out_shape=jax.ShapeDtypeStruct((t, D), BF16),
        compiler_params=_cp("parallel"),
    )(*z, nw, modc, modx)


def _hg_lb(lb_ref, d):
    a0 = lb_ref[0, d:d + 1, :]
    a1 = lb_ref[1, d:d + 1, :]
    mx = jnp.maximum(a0, a1)
    e0 = jnp.exp(a0 - mx)
    e1 = jnp.exp(a1 - mx)
    return e0 / (e0 + e1)


def _log_sigmoid(x):
    return jnp.minimum(x, 0.0) - jnp.log(1.0 + jnp.exp(-jnp.abs(x)))


def _gates_fwd(p, hg_lb, wgk, bgk):
    t = p.shape[0]
    seg = lambda j: _rowcol(HW, MAIN0 // HW + j)

    def body(hq_ref, hi_ref, hf_ref, hb_ref, gq_ref, gk_ref, gv_ref, lr_ref, lb_ref, wgk_ref, bgk_ref,
             q_ref, v_ref, kf_ref, kb_ref, gf_ref, gb_ref):
        q_ref[:, :HW] = _silu(hq_ref[...].astype(F32)).astype(BF16)
        q_ref[:, HW:] = (gq_ref[...].astype(F32) * (DH ** -0.5)).astype(BF16)
        v_ref[:, :HW] = hi_ref[...]
        v_ref[:, HW:] = gv_ref[...]
        xg = _dot(lr_ref[...].astype(BF16), wgk_ref[...], NN) + bgk_ref[...]
        for d, (raw_ref, k_ref, g_ref) in enumerate(((hf_ref, kf_ref, gf_ref), (hb_ref, kb_ref, gb_ref))):
            lbd = _hg_lb(lb_ref, d)
            f = lbd + (1.0 - lbd) * _sig(raw_ref[...].astype(F32))
            k_ref[:, :HW] = (1.0 - f).astype(BF16)
            k_ref[:, HW:] = gk_ref[...]
            g_ref[:, :HW] = jnp.log(f)
            g_ref[:, HW:] = _log_sigmoid(xg[:, d * HW:(d + 1) * HW]) * (1.0 / GLA_NORM)

    out = jax.ShapeDtypeStruct((t, D), F32)
    outb = jax.ShapeDtypeStruct((t, D), BF16)
    return pl.pallas_call(
        body, name="gates_fwd", grid=(t // TR,),
        in_specs=[seg(0), seg(1), seg(2), seg(3), seg(5), seg(6), seg(7), _rowcol(DH, LR0 // DH),
                  _full((2, 2, HW)), _full((DH, D)), _full((1, D))],
        out_specs=[_row(D)] * 6,
        out_shape=[outb] * 4 + [out] * 2,
        compiler_params=_cp("parallel"),
    )(p, p, p, p, p, p, p, p, hg_lb, wgk, bgk)


def _post_fwd(o_fw, o_bw, p, onw):
    t = o_fw.shape[0]

    def body(of_ref, ob_ref, g1_ref, g2_ref, w_ref, y_ref):
        for h in range(NH):
            sl = slice(h * DH, (h + 1) * DH)
            o = of_ref[:, sl] + ob_ref[:, sl]
            g_ref = g1_ref if h < NH // 2 else g2_ref
            gs = slice((h % (NH // 2)) * DH, (h % (NH // 2) + 1) * DH)
            n = o * _rstd(o) * w_ref[:, sl]
            y_ref[:, sl] = (n * _silu(g_ref[:, gs].astype(F32))).astype(BF16)

    return pl.pallas_call(
        body, name="post_fwd", grid=(t // TR,),
        in_specs=[_row(D), _row(D), _rowcol(HW, MAIN0 // HW + 4), _rowcol(HW, MAIN0 // HW + 8), _full((1, D))],
        out_specs=_row(D),
        out_shape=jax.ShapeDtypeStruct((t, D), BF16),
        compiler_params=_cp("parallel"),
    )(o_fw, o_bw, p, p, onw)


def _gate_window_specs(col0):
    return [_rowcol(HW, col0 // HW), _rowcol(HW, col0 // HW + 1), _rowcol(DH, (col0 + 2 * HW) // DH)]


def _gate_window(refs):
    return jnp.concatenate([r[...].astype(F32) for r in refs], axis=1)


def _branch_merge(y, w_hg, w_gla, p):
    t = y.shape[0]

    def body(y_ref, wh_ref, wg_ref, a0, a1, a2, b0, b1, b2, u1_ref, u2_ref, m_ref):
        u1 = _dot(y_ref[:, :HW], wh_ref[...], NN)
        u2 = _dot(y_ref[:, HW:], wg_ref[...], NN)
        u1_ref[...] = u1.astype(BF16)
        u2_ref[...] = u2.astype(BF16)
        m_ref[...] = (_sig(_gate_window((a0, a1, a2))) * u1 + _sig(_gate_window((b0, b1, b2))) * u2).astype(BF16)

    out = jax.ShapeDtypeStruct((t, GW), BF16)
    return pl.pallas_call(
        body, name="branch_merge", grid=(t // TR,),
        in_specs=[_row(D), _full((HW, GW)), _full((HW, GW))] + _gate_window_specs(GATE_HG0)
        + _gate_window_specs(GATE_GLA0),
        out_specs=[_row(GW)] * 3, out_shape=[out] * 3,
        compiler_params=_cp("parallel"),
    )(y, w_hg, w_gla, p, p, p, p, p, p)


def _mid_fwd(z, y1, nw_post, nw_pre, modc, modx):
    t = y1.shape[0]

    def body(zc_ref, zx_ref, y_ref, wpo_ref, wpr_ref, mc_ref, mx_ref, z1_ref, h_ref):
        is_ctx = pl.program_id(0) < NCT
        y = y_ref[...].astype(F32)
        z1 = _z_tile(zc_ref, zx_ref, is_ctx) + _mod_row(mc_ref, mx_ref, 2, is_ctx) * (y * _rstd(y) * wpo_ref[...])
        z1_ref[...] = z1
        n = z1 * _rstd(z1) * wpr_ref[...]
        h = n * (1.0 + _mod_row(mc_ref, mx_ref, 4, is_ctx)) + _mod_row(mc_ref, mx_ref, 3, is_ctx)
        h_ref[...] = h.astype(BF16)

    return pl.pallas_call(
        body, name="mid_fwd", grid=(t // TR,),
        in_specs=_z_specs() + [_row(D), _full((1, D)), _full((1, D)), _full((8, D)), _full((8, D))],
        out_specs=[_row(D), _row(D)],
        out_shape=[jax.ShapeDtypeStruct((t, D), F32), jax.ShapeDtypeStruct((t, D), BF16)],
        compiler_params=_cp("parallel"),
    )(*z, y1, nw_post, nw_pre, modc, modx)


def _final(z1, y2, target, nw, modc, modx):
    t = z1.shape[0]

    def body(z1_ref, y_ref, tg_ref, w_ref, mc_ref, mx_ref, dz_ref, dy_ref, loss_ref, sm_ref):
        i = pl.program_id(0)
        is_ctx = i < NCT

        @pl.when(i == 0)
        def _():
            loss_ref[...] = jnp.zeros_like(loss_ref)
            sm_ref[...] = jnp.zeros_like(sm_ref)

        g = _mod_row(mc_ref, mx_ref, 5, is_ctx)
        y = y_ref[...].astype(F32)
        r = _rstd(y)
        w = w_ref[...]
        yr = y * r
        n = yr * w
        e = z1_ref[...] + g * n - tg_ref[...]
        lat = jnp.where(is_ctx, 0.0, 1.0)
        loss_ref[...] += lat * _colsum(e * e)
        dz = e * (lat / D)
        dz_ref[...] = dz
        _acc_mod(sm_ref, 0, is_ctx, _colsum(dz * n))
        dn = dz * g
        _acc_row(sm_ref, 2, _colsum(dn * yr))
        dy_ref[...] = _rms_bwd(dn * w, y, r).astype(BF16)

    return pl.pallas_call(
        body, name="final", grid=(t // TR,),
        in_specs=[_row(D), _row(D), pl.BlockSpec((TR, D), lambda i: (jnp.maximum(i - NCT, 0), 0)),
                  _full((1, D)), _full((8, D)), _full((8, D))],
        out_specs=[_row(D), _row(D), _full((1, D)), _full((8, D))],
        out_shape=[jax.ShapeDtypeStruct((t, D), F32), jax.ShapeDtypeStruct((t, D), BF16),
                   jax.ShapeDtypeStruct((1, D), F32), jax.ShapeDtypeStruct((8, D), F32)],
        compiler_params=_cp("arbitrary"),
    )(z1, y2, target, nw, modc, modx)


def _mid_bwd(dh2, dz, z1, y1, nw_post, nw_pre, modc, modx):
    t = z1.shape[0]

    def body(dh_ref, dz_ref, z1_ref, y_ref, wpo_ref, wpr_ref, mc_ref, mx_ref, dzo_ref, dy_ref, sm_ref):
        i = pl.program_id(0)
        is_ctx = i < NCT

        @pl.when(i == 0)
        def _():
            sm_ref[...] = jnp.zeros_like(sm_ref)

        dh = dh_ref[...].astype(F32)
        z1 = z1_ref[...]
        r = _rstd(z1)
        zr = z1 * r
        wpr = wpr_ref[...]
        n = zr * wpr
        _acc_mod(sm_ref, 0, is_ctx, _colsum(dh))
        _acc_mod(sm_ref, 2, is_ctx, _colsum(dh * n))
        dn = dh * (1.0 + _mod_row(mc_ref, mx_ref, 4, is_ctx))
        _acc_row(sm_ref, 6, _colsum(dn * zr))
        dz1 = dz_ref[...] + _rms_bwd(dn * wpr, z1, r)
        dzo_ref[...] = dz1
        y = y_ref[...].astype(F32)
        r1 = _rstd(y)
        yr = y * r1
        wpo = wpo_ref[...]
        g = _mod_row(mc_ref, mx_ref, 2, is_ctx)
        _acc_mod(sm_ref, 4, is_ctx, _colsum(dz1 * (yr * wpo)))
        dn1 = dz1 * g
        _acc_row(sm_ref, 7, _colsum(dn1 * yr))
        dy_ref[...] = _rms_bwd(dn1 * wpo, y, r1).astype(BF16)

    return pl.pallas_call(
        body, name="mid_bwd", grid=(t // TR,),
        in_specs=[_row(D)] * 4 + [_full((1, D)), _full((1, D)), _full((8, D)), _full((8, D))],
        out_specs=[_row(D), _row(D), _full((8, D))],
        out_shape=[jax.ShapeDtypeStruct((t, D), F32), jax.ShapeDtypeStruct((t, D), BF16),
                   jax.ShapeDtypeStruct((8, D), F32)],
        compiler_params=_cp("arbitrary"),
    )(dh2, dz, z1, y1, nw_post, nw_pre, modc, modx)


def _pre_bwd(dh1, dz, z, nw, modc, modx):
    t = dh1.shape[0]

    def body(dh_ref, dz_ref, zc_ref, zx_ref, w_ref, mc_ref, mx_ref, dzo_ref, sm_ref):
        i = pl.program_id(0)
        is_ctx = i < NCT

        @pl.when(i == 0)
        def _():
            sm_ref[...] = jnp.zeros_like(sm_ref)

        dh = dh_ref[...].astype(F32)
        x = _z_tile(zc_ref, zx_ref, is_ctx)
        r = _rstd(x)
        xr = x * r
        w = w_ref[...]
        _acc_mod(sm_ref, 0, is_ctx, _colsum(dh))
        _acc_mod(sm_ref, 2, is_ctx, _colsum(dh * (xr * w)))
        dn = dh * (1.0 + _mod_row(mc_ref, mx_ref, 1, is_ctx))
        _acc_row(sm_ref, 4, _colsum(dn * xr))
        dzo_ref[...] = dz_ref[...] + _rms_bwd(dn * w, x, r)

    return pl.pallas_call(
        body, name="pre_bwd", grid=(t // TR,),
        in_specs=[_row(D)] * 2 + _z_specs() + [_full((1, D)), _full((8, D)), _full((8, D))],
        out_specs=[pl.BlockSpec((TR, D), lambda i: (jnp.maximum(i - NCT, 0), 0)), _full((8, D))],
        out_shape=[jax.ShapeDtypeStruct((t - CTX, D), F32), jax.ShapeDtypeStruct((8, D), F32)],
        compiler_params=_cp("arbitrary"),
    )(dh1, dz, *z, nw, modc, modx)


def _branch_merge_bwd(dm, p, u1, u2, w_hg, w_gla):
    t = dm.shape[0]

    def body(dm_ref, a0, a1, a2, b0, b1, b2, u1_ref, u2_ref, wh_ref, wg_ref, du1_ref, du2_ref, dg_ref, dyh_ref, dyg_ref):
        dm_ = dm_ref[...].astype(F32)
        s1 = _sig(_gate_window((a0, a1, a2)))
        s2 = _sig(_gate_window((b0, b1, b2)))
        du1 = (dm_ * s1).astype(BF16)
        du2 = (dm_ * s2).astype(BF16)
        du1_ref[...] = du1
        du2_ref[...] = du2
        dg_ref[:, :GW] = (dm_ * u1_ref[...].astype(F32) * s1 * (1.0 - s1)).astype(BF16)
        dg_ref[:, GW:] = (dm_ * u2_ref[...].astype(F32) * s2 * (1.0 - s2)).astype(BF16)
        dyh_ref[...] = _dot(du1, wh_ref[...], NT).astype(BF16)
        dyg_ref[...] = _dot(du2, wg_ref[...], NT).astype(BF16)

    return pl.pallas_call(
        body, name="branch_merge_bwd", grid=(t // TR,),
        in_specs=[_row(GW)] + _gate_window_specs(GATE_HG0) + _gate_window_specs(GATE_GLA0)
        + [_row(GW), _row(GW), _full((HW, GW)), _full((HW, GW))],
        out_specs=[_row(GW), _row(GW), _row(2 * GW), _row(HW), _row(HW)],
        out_shape=[jax.ShapeDtypeStruct((t, GW), BF16), jax.ShapeDtypeStruct((t, GW), BF16),
                   jax.ShapeDtypeStruct((t, 2 * GW), BF16), jax.ShapeDtypeStruct((t, HW), BF16),
                   jax.ShapeDtypeStruct((t, HW), BF16)],
        compiler_params=_cp("parallel"),
    )(dm, p, p, p, p, p, p, u1, u2, w_hg, w_gla)


def _post_bwd(dy_hg, dy_gla, o_fw, o_bw, p, onw):
    t = o_fw.shape[0]

    def body(d1_ref, d2_ref, of_ref, ob_ref, g1_ref, g2_ref, w_ref, do_ref, dg_ref, sm_ref):
        @pl.when(pl.program_id(0) == 0)
        def _():
            sm_ref[...] = jnp.zeros_like(sm_ref)

        for h in range(NH):
            sl = slice(h * DH, (h + 1) * DH)
            gs = slice((h % (NH // 2)) * DH, (h % (NH // 2) + 1) * DH)
            g_ref, d_ref = (g1_ref, d1_ref) if h < NH // 2 else (g2_ref, d2_ref)
            o = of_ref[:, sl] + ob_ref[:, sl]
            r = _rstd(o)
            orr = o * r
            w = w_ref[:, sl]
            gt = g_ref[:, gs].astype(F32)
            dy = d_ref[:, gs].astype(F32)
            dg_ref[:, sl] = (dy * (orr * w) * _dsilu(gt)).astype(BF16)
            dn = dy * _silu(gt)
            sm_ref[0:1, sl] += _colsum(dn * orr)
            do_ref[:, sl] = _rms_bwd(dn * w, o, r)

    return pl.pallas_call(
        body, name="post_bwd", grid=(t // TR,),
        in_specs=[_row(HW), _row(HW), _row(D), _row(D), _rowcol(HW, MAIN0 // HW + 4), _rowcol(HW, MAIN0 // HW + 8),
                  _full((1, D))],
        out_specs=[_row(D), _row(D), _full((8, D))],
        out_shape=[jax.ShapeDtypeStruct((t, D), F32), jax.ShapeDtypeStruct((t, D), BF16),
                   jax.ShapeDtypeStruct((8, D), F32)],
        compiler_params=_cp("arbitrary"),
    )(dy_hg, dy_gla, o_fw, o_bw, p, p, onw)


def _gates_bwd(p, hg_lb, wgk, bgk, dgm, dgo, dq_f, dq_b, dv_f, dv_b, dk_f, dk_b, dg_f, dg_b):
    t = p.shape[0]
    seg = lambda j: _rowcol(HW, MAIN0 // HW + j)

    def body(hq_ref, hf_ref, hb_ref, lr_ref, lb_ref, wgk_ref, bgk_ref, dgm_ref, dgo_ref,
             dqf_ref, dqb_ref, dvf_ref, dvb_ref, dkf_ref, dkb_ref, dgf_ref, dgb_ref,
             dp_ref, dlb_ref, dw_ref, db_ref):
        @pl.when(pl.program_id(0) == 0)
        def _():
            dlb_ref[...] = jnp.zeros_like(dlb_ref)
            dw_ref[...] = jnp.zeros_like(dw_ref)
            db_ref[...] = jnp.zeros_like(db_ref)

        c0 = MAIN0

        def put(j, val):
            dp_ref[:, c0 + j * HW:c0 + (j + 1) * HW] = val.astype(BF16)

        dq = dqf_ref[...].astype(F32) + dqb_ref[...].astype(F32)
        dv = dvf_ref[...].astype(F32) + dvb_ref[...].astype(F32)
        put(0, dq[:, :HW] * _dsilu(hq_ref[...].astype(F32)))
        put(1, dv[:, :HW])
        put(5, dq[:, HW:] * (DH ** -0.5))
        put(7, dv[:, HW:])
        put(6, dkf_ref[:, HW:].astype(F32) + dkb_ref[:, HW:].astype(F32))
        dp_ref[:, c0 + 4 * HW:c0 + 5 * HW] = dgo_ref[:, :HW]
        dp_ref[:, c0 + 8 * HW:c0 + 9 * HW] = dgo_ref[:, HW:]
        lr = lr_ref[...].astype(BF16)
        xg = _dot(lr, wgk_ref[...], NN) + bgk_ref[...]
        dxg = []
        for d, (raw_ref, dk_ref, dg_ref) in enumerate(((hf_ref, dkf_ref, dgf_ref), (hb_ref, dkb_ref, dgb_ref))):
            lbd = _hg_lb(lb_ref, d)
            s = _sig(raw_ref[...].astype(F32))
            f = lbd + (1.0 - lbd) * s
            df = dg_ref[:, :HW] / f - dk_ref[:, :HW].astype(F32)
            put(2 + d, df * (1.0 - lbd) * s * (1.0 - s))
            dlb_ref[d:d + 1, :] += _colsum(df * (1.0 - s)) * (lbd * (1.0 - lbd))
            dxg.append(dg_ref[:, HW:] * (1.0 / GLA_NORM) * _sig(-xg[:, d * HW:(d + 1) * HW]))
        dxg = jnp.concatenate(dxg, axis=1)
        db_ref[0:1, :] += _colsum(dxg)
        dxg_b = dxg.astype(BF16)
        dw_ref[...] += _dot(lr, dxg_b, TN)
        dlr = _dot(dxg_b, wgk_ref[...], NT)
        dp_ref[:, LR0:LR0 + DH] = (dlr + dgm_ref[:, :DH].astype(F32)).astype(BF16)
        dp_ref[:, LR0 + DH:GATE_GLA0] = dgm_ref[:, DH:D]
        dp_ref[:, GATE_GLA0:GATE_GLA0 + DH] = dgm_ref[:, D:GW] + dgm_ref[:, GW:GW + DH]
        dp_ref[:, GATE_GLA0 + DH:GATE_GLA0 + GW] = dgm_ref[:, GW + DH:]
        dp_ref[:, GATE_GLA0 + GW:] = jnp.zeros((TR, W_IN_COLS - GATE_GLA0 - GW), BF16)

    return pl.pallas_call(
        body, name="gates_bwd", grid=(t // TR,),
        in_specs=[seg(0), seg(2), seg(3), _rowcol(DH, LR0 // DH), _full((2, 2, HW)), _full((DH, D)), _full((1, D)),
                  _row(2 * GW), _row(D)] + [_row(D)] * 8,
        out_specs=[_row(W_IN_COLS), _full((8, HW)), _full((DH, D)), _full((8, D))],
        out_shape=[jax.ShapeDtypeStruct((t, W_IN_COLS), BF16), jax.ShapeDtypeStruct((8, HW), F32),
                   jax.ShapeDtypeStruct((DH, D), F32), jax.ShapeDtypeStruct((8, D), F32)],
        compiler_params=_cp("arbitrary"),
    )(p, p, p, p, hg_lb, wgk, bgk, dgm, dgo, dq_f, dq_b, dv_f, dv_b, dk_f, dk_b, dg_f, dg_b)


def _scan_consts(rev):
    r = lax.broadcasted_iota(jnp.int32, (CHUNK, CHUNK), 0)
    u = lax.broadcasted_iota(jnp.int32, (CHUNK, CHUNK), 1)
    rp = lax.broadcasted_iota(jnp.int32, (CHUNK, 1), 0)
    if rev:
        r, u, rp = CHUNK - 1 - r, CHUNK - 1 - u, CHUNK - 1 - rp
    tri = jnp.where(u <= r, 1.0, 0.0).astype(F32)
    tri_t = jnp.where(r <= u, 1.0, 0.0).astype(F32)
    lv = []
    for b in LEVELS:
        sh = b.bit_length() - 1
        pair = ((r >> sh) == (u >> sh) + 1) & (((u >> sh) & 1) == 0)
        pair_t = ((u >> sh) == (r >> sh) + 1) & (((r >> sh) & 1) == 0)
        tside = ((rp >> sh) & 1) == 1
        lv.append((pair, pair_t, tside, jnp.where(tside, 1.0, -1.0).astype(F32)))
    bd = LEVELS[-1].bit_length() - 1
    diag = ((r >> bd) == (u >> bd)) & (u <= r)
    diag_t = ((r >> bd) == (u >> bd)) & (r <= u)
    return tri, tri_t, lv, diag, diag_t


def _row_of(pos, rev):
    return CHUNK - 1 - pos if rev else pos


def _chunk_terms(cum, b_scr, consts, rev):
    _, _, lv, _, _ = consts
    terms = []
    for b, (_, _, _, sgn) in zip(LEVELS, lv):
        pieces = []
        for j in range(CHUNK // (2 * b)):
            row = _row_of(2 * b * j + b - 1, rev)
            pieces.append(jnp.broadcast_to(b_scr[row:row + 1, :], (2 * b, DH)))
        if rev:
            pieces = pieces[::-1]
        bnd = pieces[0] if len(pieces) == 1 else jnp.concatenate(pieces, axis=0)
        terms.append(jnp.exp((cum - bnd) * sgn))
    b = LEVELS[-1]
    pieces = []
    for j in range(CHUNK // b):
        if j == 0:
            pieces.append(jnp.zeros((b, DH), F32))
        else:
            row = _row_of(b * j - 1, rev)
            pieces.append(jnp.broadcast_to(b_scr[row:row + 1, :], (b, DH)))
    if rev:
        pieces = pieces[::-1]
    start = jnp.concatenate(pieces, axis=0)
    wq = jnp.exp(jnp.minimum(cum - start, 0.0))
    wk = jnp.exp(jnp.minimum(start - cum, EXP_CLAMP))
    terms.append((wq, wk))
    return terms


def _run_staged(units):
    live = list(units)
    while live:
        nxt = []
        for u in live:
            try:
                next(u)
                nxt.append(u)
            except StopIteration:
                pass
        live = nxt


SCAN_TB = 256
SCAN_CB = SCAN_TB // CHUNK


def _block_order(i, ntb, rev):
    nctx = CTX // SCAN_TB
    if not rev:
        return i
    return jnp.where(i < nctx, nctx - 1 - i, ntb - 1 - (i - nctx))


def _chunk_in_block(j, rev):
    return SCAN_CB - 1 - j if rev else j


def _scan_fwd(q, k, v, g, rev):
    t = q.shape[0]
    nc = t // CHUNK
    hpb = SCAN_HEADS_FWD

    def body(q_ref, k_ref, v_ref, g_ref, o_ref, st_ref, s_scr, b_scr):
        consts = _scan_consts(rev)
        _, _, lv, diag, _ = consts
        masks = [lvl[0] for lvl in lv] + [diag]

        @pl.when(pl.program_id(1) == 0)
        def _():
            s_scr[...] = jnp.zeros_like(s_scr)

        tri = consts[0]
        state = {hh: s_scr[hh] for hh in range(hpb)}

        def unit(hh, j):
            sl = slice(hh * DH, (hh + 1) * DH)
            c = _chunk_in_block(j, rev)
            rows = slice(c * CHUNK, (c + 1) * CHUNK)
            b_ref = b_scr.at[hh * SCAN_CB + j]
            qc, kc, vc, gc = q_ref[rows, sl], k_ref[rows, sl], v_ref[rows, sl], g_ref[rows, sl]
            cum = _split_dot(tri, gc)
            b_ref[...] = cum
            yield
            terms = _chunk_terms(cum, b_ref, consts, rev)
            qf, kf = qc.astype(F32), kc.astype(F32)
            xs = [(jnp.where(tside, qf, kf) * w).astype(BF16) for w, (_, _, tside, _) in zip(terms[:-1], lv)]
            qd, kd = (qf * terms[-1][0]).astype(BF16), (kf * terms[-1][1]).astype(BF16)
            tot = _colsum(gc)
            qe = (qf * jnp.exp(cum)).astype(BF16)
            ke = (kf * jnp.exp(tot - cum)).astype(BF16)
            vb = vc.astype(BF16)
            yield
            scs = [_dot(x, x, NT) for x in xs] + [_dot(qd, kd, NT)]
            kv = _dot(vb, ke, TN)
            yield
            a = jnp.zeros((CHUNK, CHUNK), F32)
            for sc, m in zip(scs, masks):
                a = a + jnp.where(m, sc, 0.0)
            o_intra = _dot(a.astype(BF16), vb, NN)
            yield
            st = state[hh]
            st_ref[hh, c] = st
            o_ref[rows, sl] = o_intra + _dot(qe, st.astype(BF16), NT)
            state[hh] = st * jnp.exp(tot) + kv
            yield

        _run_staged([unit(hh, j) for hh in range(hpb) for j in range(SCAN_CB)])
        for hh in range(hpb):
            s_scr[hh] = state[hh]

    ntb = t // SCAN_TB
    col = pl.BlockSpec((SCAN_TB, hpb * DH), lambda h, i: (_block_order(i, ntb, rev), h))
    return pl.pallas_call(
        body, name="scan_fwd_" + ("bw" if rev else "fw"), grid=(NH // hpb, ntb),
        in_specs=[col] * 4,
        out_specs=[col, pl.BlockSpec((hpb, SCAN_CB, DH, DH), lambda h, i: (h, _block_order(i, ntb, rev), 0, 0))],
        out_shape=[jax.ShapeDtypeStruct((t, D), F32), jax.ShapeDtypeStruct((NH, nc, DH, DH), F32)],
        scratch_shapes=[pltpu.VMEM((hpb, DH, DH), F32), pltpu.VMEM((hpb * SCAN_CB, CHUNK, DH), F32)],
        compiler_params=_cp("parallel", "arbitrary"),
    )(q, k, v, g)


def _scan_bwd(q, k, v, g, do, states, rev):
    t = q.shape[0]
    nc = t // CHUNK
    hpb = SCAN_HEADS_BWD

    def body(q_ref, k_ref, v_ref, g_ref, do_ref, st_ref, dq_ref, dk_ref, dv_ref, dg_ref, ds_scr, b_scr):
        consts = _scan_consts(rev)
        _, tri_t, lv, diag, diag_t = consts
        masks = [(lvl[0], lvl[1]) for lvl in lv] + [(diag, diag_t)]
        @pl.when(pl.program_id(1) == 0)
        def _():
            ds_scr[...] = jnp.zeros_like(ds_scr)

        tri = consts[0]
        dstate = {hh: ds_scr[hh] for hh in range(hpb)}

        def unit(hh, jj):
            sl = slice(hh * DH, (hh + 1) * DH)
            c = _chunk_in_block(SCAN_CB - 1 - jj, rev)
            rows = slice(c * CHUNK, (c + 1) * CHUNK)
            b_ref = b_scr.at[hh * SCAN_CB + jj]
            qc, kc, vc, gc = q_ref[rows, sl], k_ref[rows, sl], v_ref[rows, sl], g_ref[rows, sl]
            dob = do_ref[rows, sl].astype(BF16)
            vb = vc.astype(BF16)
            cum = _split_dot(tri, gc)
            b_ref[...] = cum
            da = _dot(dob, vb, NT)
            da_t = _dot(vb, dob, NT)
            yield
            terms = _chunk_terms(cum, b_ref, consts, rev)
            qf, kf = qc.astype(F32), kc.astype(F32)
            xs = [(jnp.where(tside, qf, kf) * w).astype(BF16) for w, (_, _, tside, _) in zip(terms[:-1], lv)]
            wqd, wkd = terms[-1]
            qdb, kdb = (qf * wqd).astype(BF16), (kf * wkd).astype(BF16)
            tot = _colsum(gc)
            e_tot = jnp.exp(tot)
            e_b = jnp.exp(cum)
            e_t = jnp.exp(tot - cum)
            qeb = (qf * e_b).astype(BF16)
            keb = (kf * e_t).astype(BF16)
            dsym = [(jnp.where(m, da, 0.0) + jnp.where(m_t, da_t, 0.0)).astype(BF16) for m, m_t in masks[:-1]]
            dad = (jnp.where(diag, da, 0.0).astype(BF16), jnp.where(diag_t, da_t, 0.0).astype(BF16))
            yield
            sym = [_dot(x, x, NT) for x in xs]
            dxs = [_dot(d, x, NN) for d, x in zip(dsym, xs)]
            at_d = _dot(kdb, qdb, NT)
            dqt_d = _dot(dad[0], kdb, NN)
            dkt_d = _dot(dad[1], qdb, NN)
            qd = _dot(dob, qeb, TN)
            yield
            a_t = jnp.where(diag_t, at_d, 0.0)
            dq = dqt_d * wqd
            dk = dkt_d * wkd
            db = dqt_d * qdb.astype(F32) - dkt_d * kdb.astype(F32)
            for s, dx, x, w, (_, m_t, tside, sgn) in zip(sym, dxs, xs, terms[:-1], lv):
                a_t = a_t + jnp.where(m_t, s, 0.0)
                dxw = dx * w
                dq = dq + jnp.where(tside, dxw, 0.0)
                dk = dk + jnp.where(tside, 0.0, dxw)
                db = db + (dx * x.astype(F32)) * sgn
            dv_intra = _dot(a_t.astype(BF16), dob, NN)
            st = st_ref[hh, c]
            stb = st.astype(BF16)
            dqe = _dot(dob, stb, NN)
            yield
            dst = dstate[hh]
            dstb = dst.astype(BF16)
            dstate[hh] = dst * e_tot + qd
            dv_ref[rows, sl] = (dv_intra + _dot(keb, dstb, NT)).astype(BF16)
            dke = _dot(vb, dstb, NN)
            yield
            qe = qeb.astype(F32)
            ke = keb.astype(F32)
            dq_ref[rows, sl] = (dq + dqe * e_b).astype(BF16)
            dk_ref[rows, sl] = (dk + dke * e_t).astype(BF16)
            db = db + dqe * qe - dke * ke
            dtot = _colsum(dstb.astype(F32) * stb.astype(F32)) * e_tot + _colsum(dke * ke)
            dg_ref[rows, sl] = _split_dot(tri_t, db) + dtot
            yield

        _run_staged([unit(hh, jj) for hh in range(hpb) for jj in range(SCAN_CB)])
        for hh in range(hpb):
            ds_scr[hh] = dstate[hh]

    ntb = t // SCAN_TB
    blk = lambda i: _block_order(ntb - 1 - i, ntb, rev)
    col = pl.BlockSpec((SCAN_TB, hpb * DH), lambda h, i: (blk(i), h))
    out = jax.ShapeDtypeStruct((t, D), F32)
    outb = jax.ShapeDtypeStruct((t, D), BF16)
    return pl.pallas_call(
        body, name="scan_bwd_" + ("bw" if rev else "fw"), grid=(NH // hpb, ntb),
        in_specs=[col] * 5 + [pl.BlockSpec((hpb, SCAN_CB, DH, DH), lambda h, i: (h, blk(i), 0, 0))],
        out_specs=[col] * 4,
        out_shape=[outb] * 3 + [out],
        scratch_shapes=[pltpu.VMEM((hpb, DH, DH), F32), pltpu.VMEM((hpb * SCAN_CB, CHUNK, DH), F32)],
        compiler_params=_cp("parallel", "arbitrary"),
    )(q, k, v, g, do, states)


W_IN_GRAD_CHUNKS = (("a", (0, 512)), ("b", (0, 256)), ("b", (256, 512)))
W_IN_REF = 6688
W_IN_PAD = 896
W_IN_PIECE = 256
W_IN_STAGES = (3, 4)


def _assemble_w_in(g, rows, prev, name):
    n, r, wp = g.shape
    tr = W_IN_PIECE
    tiles = wp // DH
    first = rows[0] // tr

    def body(g_ref, *refs):
        o_ref = refs[-1]
        lane = lax.broadcasted_iota(jnp.int32, (tr, DH), 1)
        for t in range(W_IN_COLS // DH):
            acc = None
            for j in range(n):
                c = DH * t - W_IN_SHARD * j
                if c <= -DH or c >= W_IN_SHARD:
                    continue
                k, s = divmod(c, DH)
                lo = g_ref[j, :, k * DH:(k + 1) * DH] if 0 <= k < tiles else None
                hi = g_ref[j, :, (k + 1) * DH:(k + 2) * DH] if s and 0 <= k + 1 < tiles else None
                if s:
                    zero = jnp.zeros((tr, DH), g.dtype)
                    lo = zero if lo is None else pltpu.roll(lo, DH - s, 1)
                    hi = zero if hi is None else pltpu.roll(hi, DH - s, 1)
                    part = jnp.where(lane < DH - s, lo, hi)
                else:
                    part = lo
                acc = part if acc is None else acc + part
            o_ref[:, t * DH:(t + 1) * DH] = jnp.zeros((tr, DH), g.dtype) if acc is None else acc

    held = [] if prev is None else [prev]
    return pl.pallas_call(
        body, name=name, grid=((rows[1] - rows[0]) // tr,),
        in_specs=[pl.BlockSpec((n, tr, wp), lambda i: (0, first + i, 0))] + [pl.BlockSpec(memory_space=pl.ANY)] * len(held),
        out_specs=pl.BlockSpec((tr, W_IN_COLS), lambda i: (first + i, 0)),
        out_shape=jax.ShapeDtypeStruct((r, W_IN_COLS), g.dtype),
        input_output_aliases={1: 0} if held else {},
        compiler_params=_cp("parallel"),
    )(g, *held)


def _gate_cols(w):
    return jnp.pad(w, ((0, 0), (GOFF, GW - GOFF - D)))


def _gate_rows(w):
    return jnp.pad(w, ((GOFF, GW - GOFF - D), (0, 0)))


def _layout_wgk(w):
    r = w.shape[1]
    top = jnp.concatenate([w[0], jnp.zeros_like(w[0])], axis=1)
    bot = jnp.concatenate([jnp.zeros_like(w[1]), w[1]], axis=1)
    return jnp.concatenate([top, bot, jnp.zeros((DH - 2 * r, D), w.dtype)], axis=0)


def _unlayout_wgk(d, r=16):
    return jnp.stack([d[:r, :HW], d[r:2 * r, HW:]])


def _local_step(z, target, modc, modx, norms, onw, hg_lb, wgk, bgk, get_w_in, get_mix, get_ffn, send):
    n_pre1, n_post1, n_pre2, n_post2 = norms
    t = z[0].shape[0] + z[1].shape[0]
    tm = 1152 if t % 1152 == 0 else 256
    h1 = _prenorm(z, n_pre1, modc, modx, 0, 1, "prenorm1")
    w_in = get_w_in(h1)
    p = _matmul(h1, w_in, NN, BF16, "mm_in", t, 1024, D)
    q, v, k_f, k_b, g_f, g_b = _gates_fwd(p, hg_lb, wgk, bgk)
    o_f, st_f = _scan_fwd(q, k_f, v, g_f, False)
    o_b, st_b = _scan_fwd(q, k_b, v, g_b, True)
    y = _post_fwd(o_f, o_b, p, onw)
    w_br_hg, w_br_gla, w_out = get_mix(y)
    u1, u2, merged = _branch_merge(y, w_br_hg, w_br_gla, p)
    y1 = _matmul(merged, w_out, NN, BF16, "mm_out", tm, 512, GW)
    z1, h2 = _mid_fwd(z, y1, n_post1, n_pre2, modc, modx)
    w_gu_t, get_down = get_ffn(h2)
    u, v_ff, act = _mm_gu_act(h2, w_gu_t[0], w_gu_t[1], "mm_gu", tm)
    w_down = get_down(act)
    y2 =_matmul(act, w_down, NN, BF16, "mm_down", t, 512, D_FF)
    dz, dy2, loss_vec, sm_final = _final(z1, y2, target, n_post2, modc, modx)
    du, dv_ff = _mm_down_dx_act(dy2, w_down, u, v_ff, "mm_down_dx", tm)
    d_w_down = _matmul(act, dy2, TN, BF16, "mm_down_dw", D_FF // 2, 1024, t)
    dh2 = _matmul((du, dv_ff), w_gu_t, NN, BF16, "mm_gu_dx", tm, 512, D_FF)
    d_w_gate_t = _matmul(du, h2, TN, BF16, "mm_gate_dw", D_FF // 2, 1024, t)
    d_w_up_t = _matmul(dv_ff, h2, TN, BF16, "mm_up_dw", D_FF // 2, 1024, t)
    dh2 = send(("w_down", "w_gate_t", "w_up_t"), (d_w_down, d_w_gate_t, d_w_up_t), dh2)
    dz, dy1, sm_mid = _mid_bwd(dh2, dz, z1, y1, n_post1, n_pre2, modc, modx)
    dmerged = _matmul(dy1, w_out, NT, BF16, "mm_out_dx", tm, GW, D)
    d_w_out = _matmul(merged, dy1, TN, BF16, "mm_out_dw", GW, 512, t)
    du1, du2, dgm, dy_hg, dy_gla = _branch_merge_bwd(dmerged, p, u1, u2, w_br_hg, w_br_gla)
    d_w_br_hg = _matmul(y, du1, TN, BF16, "mm_br_hg_dw", HW, GW, t, a_off=0, m_out=HW)
    d_w_br_gla = _matmul(y, du2, TN, BF16, "mm_br_gla_dw", HW, GW, t, a_off=1, m_out=HW)
    dy_hg = send(("w_out", "w_br_hg", "w_br_gla"), (d_w_out, d_w_br_hg, d_w_br_gla), dy_hg)
    do, dgo, sm_post = _post_bwd(dy_hg, dy_gla, o_f, o_b, p, onw)
    dq_f, dk_f, dv_f, dg_f = _scan_bwd(q, k_f, v, g_f, do, st_f, False)
    dq_b, dk_b, dv_b, dg_b = _scan_bwd(q, k_b, v, g_b, do, st_b, True)
    dp, d_lb, d_wgk, d_bgk = _gates_bwd(p, hg_lb, wgk, bgk, dgm, dgo, dq_f, dq_b, dv_f, dv_b, dk_f, dk_b, dg_f, dg_b)
    d_w_in_a = _matmul(h1, dp, TN, BF16, "mm_in_dw_a", 512, 1024, t, a_off=0, m_out=D // 2)
    dp = send(("w_in_a",), (d_w_in_a,), dp)
    d_w_in_b = _matmul(h1, dp, TN, BF16, "mm_in_dw_b", 512, 1024, t, a_off=1, m_out=D // 2)
    dp = send(("w_in_b",), (d_w_in_b,), dp)
    dh1 = _matmul(dp, w_in, NT, BF16, "mm_in_dx", tm, 512, W_IN_COLS // 2)
    grad_x, sm_pre = _pre_bwd(dh1, dz, z, n_pre1, modc, modx)
    return dict(loss_vec=loss_vec, grad_x=grad_x, sm_final=sm_final, sm_mid=sm_mid, sm_post=sm_post, sm_pre=sm_pre,
                d_lb=d_lb, d_wgk=d_wgk, d_bgk=d_bgk)


MESH = pl.DeviceIdType.MESH
ANY = pl.BlockSpec(memory_space=pl.ANY)
N_REL = N_DEV - 1


def _place():
    return lax.axis_index("x"), lax.axis_index("y"), lax.axis_index("c")


def _slot(p):
    return 4 * p[0] + 2 * p[1] + p[2]


HBM = pl.BlockSpec(memory_space=pltpu.HBM)
SEM = pl.BlockSpec(memory_space=pltpu.SEMAPHORE)
EFFECT = pltpu.SideEffectType.DATAFLOW_SIDE_EFFECTING


def _peer_of(x, y, c, k):
    flip = lambda v, bit: 1 - v if bit else v
    return flip(x, k & 4), flip(y, k & 2), flip(c, k & 1)


def _view_whole(src, slot):
    return src


def _view_near(src, slot):
    return src


_view_near.peers = (1, 2, 4, 6)


def _view_near_rows(rows):
    def view(src, slot):
        return src.at[pl.ds(rows[0], rows[1] - rows[0])]
    view.peers = _view_near.peers
    view.land = lambda land, slot: land.at[slot, pl.ds(rows[0], rows[1] - rows[0])]
    return view


def _view_block(src, slot):
    return src.at[slot]


def _view_cols(src, slot):
    return src.at[:, pl.ds(pl.multiple_of(slot * (D // N_DEV), D // N_DEV), D // N_DEV)]


W_IN_SHARD = W_IN_REF // N_DEV


def _view_window(rows):
    def view(src, slot):
        col0 = pl.multiple_of((W_IN_SHARD * slot // DH) * DH, DH)
        return src.at[pl.ds(rows[0], rows[1] - rows[0]), pl.ds(col0, D)]
    return view


def _split_copies(view, srcs, lands, send_sems, recv_sems, local_sems):
    x, y, c = _place()
    me = _slot((x, y, c))
    into = getattr(view, "land", lambda land, slot: land.at[slot])
    local, sends, waits = [], [], []
    for a, (src, land) in enumerate(zip(srcs, lands)):
        local.append(pltpu.make_async_copy(view(src, me), into(land, me), local_sems.at[a]))
        for k in getattr(view, "peers", range(1, N_DEV)):
            peer = _peer_of(x, y, c, k)
            mine = view(src, _slot(peer))
            sems = dict(send_sem=send_sems.at[N_REL * a + k - 1], recv_sem=recv_sems.at[N_REL * a + k - 1],
                        device_id=peer, device_id_type=MESH)
            sends.append(pltpu.make_async_remote_copy(src_ref=mine, dst_ref=into(land, me), **sems))
            waits.append(pltpu.make_async_remote_copy(src_ref=mine, dst_ref=into(land, _slot(peer)), **sems))
    return local, sends, waits


def _split_start(groups, name, after):
    built = []
    for view, srcs, lands in groups:
        lands = [lax.empty(l, s.dtype) if isinstance(l, tuple) else l for l, s in zip(lands, srcs)]
        built.append((view, list(srcs), lands))
    bufs = [b for _, srcs, lands in built for b in srcs + lands]
    nb, ng = len(bufs), len(built)

    def body(*refs):
        buf_refs, sem_refs, token = refs[:nb], refs[nb + 1:nb + 1 + 3 * ng], refs[-1]
        pos = 0
        for i, (view, srcs, _) in enumerate(built):
            n = len(srcs)
            local, sends, _ = _split_copies(view, buf_refs[pos:pos + n], buf_refs[pos + n:pos + 2 * n],
                                            *sem_refs[3 * i:3 * i + 3])
            pos += 2 * n
            for cp in local + sends:
                cp.start()
        token[...] = jnp.zeros_like(token)

    sems = []
    for _, srcs, _ in built:
        n = len(srcs)
        sems += [pltpu.SemaphoreType.DMA((N_REL * n,)), pltpu.SemaphoreType.DMA((N_REL * n,)),
                 pltpu.SemaphoreType.DMA((n,))]
    hbm = lambda a: pltpu.with_memory_space_constraint(a, pltpu.HBM)
    out = pl.pallas_call(
        body, name=name,
        out_shape=(*sems, *[pltpu.HBM(b.shape, b.dtype) for b in bufs], jax.ShapeDtypeStruct((8, DH), F32)),
        in_specs=[HBM] * nb + [ANY],
        out_specs=(*([SEM] * (3 * ng)), *([HBM] * nb), pl.BlockSpec(memory_space=pltpu.VMEM)),
        input_output_aliases={i: 3 * ng + i for i in range(nb)},
        compiler_params=pltpu.CompilerParams(has_side_effects=EFFECT),
    )(*[hbm(b) for b in bufs], after)
    handles, pos = [], 3 * ng
    for i, (view, srcs, _) in enumerate(built):
        n = len(srcs)
        handles.append(dict(view=view, n=n, sems=out[3 * i:3 * i + 3], srcs=list(out[pos:pos + n]),
                            lands=list(out[pos + n:pos + 2 * n])))
        pos += 2 * n
    return handles, out[-1]


def _split_wait(handle, name, after, srcs=None, lands=None):
    view, n, sems = handle["view"], handle["n"], handle["sems"]
    srcs = handle["srcs"] if srcs is None else srcs
    lands = handle["lands"] if lands is None else lands
    afters = list(after) if isinstance(after, (list, tuple)) else [after]

    def body(*refs):
        src_refs, land_refs = refs[:n], refs[n:2 * n]
        send_sems, recv_sems, local_sems = refs[2 * n:2 * n + 3]
        local, _, waits = _split_copies(view, src_refs, land_refs, send_sems, recv_sems, local_sems)
        for cp in waits:
            cp.wait_send()
            cp.wait_recv()
        for cp in local:
            cp.wait()

    out = pl.pallas_call(
        body, name=name,
        out_shape=(*[pltpu.HBM(s.shape, s.dtype) for s in srcs], *[pltpu.HBM(l.shape, l.dtype) for l in lands]),
        in_specs=[HBM] * (2 * n) + [SEM, SEM, SEM] + [ANY] * len(afters),
        out_specs=tuple([HBM] * (2 * n)),
        input_output_aliases={i: i for i in range(2 * n)},
        compiler_params=pltpu.CompilerParams(has_side_effects=EFFECT),
    )(*srcs, *lands, *sems, *afters)
    handle["srcs"] = list(out[:n])
    return list(out[n:])


def _tie(x, token, name):
    def body(x_ref, t_ref, o_ref):
        pass

    return pl.pallas_call(
        body, name=name, out_shape=jax.ShapeDtypeStruct(x.shape, x.dtype),
        in_specs=[ANY, ANY], out_specs=ANY, input_output_aliases={0: 0},
    )(x, token)


def _forward_to_sibling(land, name, rows):
    def body(land_ref, out_ref, send_sems, recv_sems):
        x, y, c = _place()
        sibling = (x, y, 1 - c)
        chips = [(1 - x, y), (x, 1 - y), (1 - x, 1 - y)]
        piece = pl.ds(rows[0], rows[1] - rows[0])

        def copy(j, core):
            blk = _slot((*chips[j], core))
            return pltpu.make_async_remote_copy(src_ref=land_ref.at[blk, piece], dst_ref=out_ref.at[blk, piece],
                                                send_sem=send_sems.at[j], recv_sem=recv_sems.at[j],
                                                device_id=sibling, device_id_type=MESH)

        sends = [copy(j, c) for j in range(3)]
        for cp in sends:
            cp.start()
        for j in range(3):
            copy(j, 1 - c).wait_recv()
        for cp in sends:
            cp.wait_send()

    return pl.pallas_call(
        body, name=name, in_specs=[ANY], out_specs=ANY, input_output_aliases={0: 0},
        out_shape=jax.ShapeDtypeStruct(land.shape, land.dtype),
        scratch_shapes=[pltpu.SemaphoreType.DMA((3,)), pltpu.SemaphoreType.DMA((3,))],
    )(land)


def _mod_fwd(a, w, b):
    def body(a_ref, w_ref, b_ref, o_ref):
        o_ref[...] = _dot(_silu(a_ref[...]), w_ref[...], NN, precision=HI) + b_ref[...]

    return pl.pallas_call(
        body, name="mod_fwd", out_shape=jax.ShapeDtypeStruct((a.shape[0], w.shape[1]), F32),
        compiler_params=pltpu.CompilerParams(vmem_limit_bytes=VMEM_LIMIT),
    )(a, w, b)


def _mod_bwd(a, d, w):
    def body(a_ref, d_ref, w_ref, dw_ref, dc_ref):
        av = a_ref[...]
        dv = d_ref[...]
        dw_ref[...] = _dot(_silu(av), dv, TN, precision=HI)
        da = _dot(dv[0:8, :], w_ref[...], NT, precision=HI) * _dsilu(av[0:8, :])
        row = lax.broadcasted_iota(jnp.int32, da.shape, 0)
        dc_ref[...] = jnp.where(row == 0, da, 0.0)

    return pl.pallas_call(
        body, name="mod_bwd",
        out_shape=[jax.ShapeDtypeStruct(w.shape, F32), jax.ShapeDtypeStruct((8, w.shape[0]), F32)],
        compiler_params=pltpu.CompilerParams(vmem_limit_bytes=VMEM_LIMIT),
    )(a, d, w)


def _sum_devices(g):
    def body(g_ref, o_ref):
        acc = g_ref[0]
        for i in range(1, g.shape[0]):
            acc = acc + g_ref[i]
        o_ref[...] = acc

    return pl.pallas_call(body, name="sum_devices_%d" % g.shape[1],
                          out_shape=jax.ShapeDtypeStruct(g.shape[1:], F32))(g)


def _sum_windows(g, name):
    n, r, c = g.shape
    tr = 128

    def body(g_ref, o_ref):
        x, y, cc = _place()
        lane0 = (W_IN_SHARD * _slot((x, y, cc))) % DH
        acc = g_ref[0].astype(F32)
        for i in range(1, n):
            acc = acc + g_ref[i].astype(F32)
        o_ref[...] = pltpu.roll(acc, (c - lane0) % c, 1).T

    return pl.pallas_call(
        body, name=name, grid=(r // tr,),
        in_specs=[pl.BlockSpec((n, tr, c), lambda i: (0, i, 0))],
        out_specs=pl.BlockSpec((c, tr), lambda i: (0, i)),
        out_shape=jax.ShapeDtypeStruct((c, r), F32),
        compiler_params=_cp("parallel"),
    )(g)


def _adam_rows(r, c, n):
    budget = 10 * 1024 * 1024
    best = None
    for tr in range(16, r + 1, 16):
        if r % tr == 0 and tr * c * (2 * n + 28) <= budget:
            best = tr
    return best if best is not None else r


def _adamw(g, w, m, v, name):
    n, r, c = g.shape
    tr = _adam_rows(r, c, n)
    bc1 = 1.0 - ADAM_B1 ** ADAM_STEP
    bc2 = 1.0 - ADAM_B2 ** ADAM_STEP

    def body(g_ref, w_ref, m_ref, v_ref, go_ref, d_ref, mo_ref, vo_ref):
        grad = g_ref[0].astype(F32)
        for i in range(1, n):
            grad = grad + g_ref[i].astype(F32)
        go_ref[...] = grad
        m_new = ADAM_B1 * m_ref[...] + (1.0 - ADAM_B1) * grad
        v_new = ADAM_B2 * v_ref[...] + (1.0 - ADAM_B2) * (grad * grad)
        mo_ref[...] = m_new
        vo_ref[...] = v_new
        d_ref[...] = -ADAM_LR * ((m_new / bc1) / (jnp.sqrt(v_new / bc2) + ADAM_EPS) + ADAM_WD * w_ref[...])

    blk = pl.BlockSpec((tr, c), lambda i: (i, 0))
    out = jax.ShapeDtypeStruct((r, c), F32)
    return pl.pallas_call(
        body, name=name, grid=(r // tr,),
        in_specs=[pl.BlockSpec((n, tr, c), lambda i: (0, i, 0)), blk, blk, blk],
        out_specs=[blk] * 4, out_shape=[out] * 4,
        compiler_params=_cp("parallel"),
    )(g, w, m, v)


ADAM_ROWS3 = 168


def _adam_math(grad, w, m, v):
    bc1 = 1.0 - ADAM_B1 ** ADAM_STEP
    bc2 = 1.0 - ADAM_B2 ** ADAM_STEP
    m_new = ADAM_B1 * m + (1.0 - ADAM_B1) * grad
    v_new = ADAM_B2 * v + (1.0 - ADAM_B2) * (grad * grad)
    delta = -ADAM_LR * ((m_new / bc1) / (jnp.sqrt(v_new / bc2) + ADAM_EPS) + ADAM_WD * w)
    return delta, m_new, v_new


def _adamw_rows3(g, w3, m3, v3, name, cols, prev):
    r, _, _ = w3.shape
    c = cols[1] - cols[0]
    n = min(-(-r // 16) * 8, ADAM_ROWS3 * D // c // 8 * 8)
    starts = list(range(0, r - n, n)) + [r - n]
    held = [] if prev is None else list(prev)

    def body(g_hbm, w_hbm, m_hbm, v_hbm, *refs):
        go_hbm, d_hbm, mo_hbm, vo_hbm, gbuf, ibuf, obuf, in_sems, out_sems = refs[len(held):]
        part = lambda h, r0: h.at[pl.ds(r0, n), 0, pl.ds(cols[0], c)]

        def fetch(p):
            r0, slot = starts[p], p % 2
            g0 = (r0 // 8) * 8
            cps = [pltpu.make_async_copy(g_hbm.at[pl.ds(g0, n + 8)], gbuf.at[slot], in_sems.at[slot, 0])]
            cps += [pltpu.make_async_copy(part(h, r0), ibuf.at[slot, k], in_sems.at[slot, 1 + k])
                    for k, h in enumerate((w_hbm, m_hbm, v_hbm))]
            for cp in cps:
                cp.start()
            return cps

        pending, outs = fetch(0), []
        for p, r0 in enumerate(starts):
            slot = p % 2
            nxt = fetch(p + 1) if p + 1 < len(starts) else []
            for cp in pending:
                cp.wait()
            grad = gbuf[slot, pl.ds(r0 - (r0 // 8) * 8, n), :]
            delta, m_new, v_new = _adam_math(grad, ibuf[slot, 0], ibuf[slot, 1], ibuf[slot, 2])
            for cp in outs:
                cp.wait()
            for k, val in enumerate((grad, delta, m_new, v_new)):
                obuf[slot, k] = val
            outs = [pltpu.make_async_copy(obuf.at[slot, k], part(h, r0), out_sems.at[slot, k])
                    for k, h in enumerate((go_hbm, d_hbm, mo_hbm, vo_hbm))]
            for cp in outs:
                cp.start()
            pending = nxt
        for cp in outs:
            cp.wait()

    out = jax.ShapeDtypeStruct(w3.shape, F32)
    return pl.pallas_call(
        body, name=name, in_specs=[ANY] * (4 + len(held)), out_specs=[ANY] * 4, out_shape=[out] * 4,
        input_output_aliases={4 + k: k for k in range(len(held))},
        scratch_shapes=[pltpu.VMEM((2, n + 8, c), F32), pltpu.VMEM((2, 3, n, c), F32), pltpu.VMEM((2, 4, n, c), F32),
                        pltpu.SemaphoreType.DMA((2, 4)), pltpu.SemaphoreType.DMA((2, 4))],
        compiler_params=pltpu.CompilerParams(vmem_limit_bytes=VMEM_LIMIT),
    )(g, w3, m3, v3, *held)


def kernel(x, c, ctx, c_ctx, w_mod, b_mod, norm_pre1, norm_post1, norm_pre2, norm_post2, w_in, hg_lb, hg_onorm, gla_w_gk, gla_b_gk, gla_onorm, w_br_hg, w_br_gla, w_out, w_ff_gate, w_ff_up, w_ff_down, loss_target, m_c_ctx, m_w_mod, m_b_mod, m_norm_pre1, m_norm_post1, m_norm_pre2, m_norm_post2, m_w_in, m_hg_lb, m_hg_onorm, m_gla_w_gk, m_gla_b_gk, m_gla_onorm, m_w_br_hg, m_w_br_gla, m_w_out, m_w_ff_gate, m_w_ff_up, m_w_ff_down, v_c_ctx, v_w_mod, v_b_mod, v_norm_pre1, v_norm_post1, v_norm_pre2, v_norm_post2, v_w_in, v_hg_lb, v_hg_onorm, v_gla_w_gk, v_gla_b_gk, v_gla_onorm, v_w_br_hg, v_w_br_gla, v_w_out, v_w_ff_gate, v_w_ff_up, v_w_ff_down):
    xi, yi, ci = lax.axis_index("x"), lax.axis_index("y"), lax.axis_index("c")
    me = 4 * xi + 2 * yi + ci
    t = CTX + x.shape[1]

    w_in_pieces, w_in_state = [], {}

    def w_in_piece(i):
        return (_view_near_rows((i * W_IN_PIECE, (i + 1) * W_IN_PIECE)), w_in_state["src"], w_in_state["land"])

    def started_w_in(handle):
        w_in_state.update(src=handle["srcs"], land=handle["lands"])
        w_in_pieces.append(handle)

    tr_ = lambda a: jnp.swapaxes(a[0], 0, 1)
    w_in_bf = jnp.pad(w_in[0].astype(BF16), ((0, 0), (0, W_IN_PAD - W_IN_SHARD)))
    w_in_state.update(src=[w_in_bf], land=[lax.empty((N_DEV,) + w_in_bf.shape, BF16)])
    gathered = lambda arrs: [(N_DEV,) + a.shape for a in arrs]
    whole = lambda arrs: (_view_whole, arrs, gathered(arrs))
    small_in = [c, hg_lb, gla_w_gk[0], gla_b_gk[0]]
    (small_handle, piece), tok = _split_start([whole(small_in), w_in_piece(0)], "ag_small_start", c)
    started_w_in(piece)
    c_all, lb_g, wgk_g, bgk_g = _split_wait(small_handle, "ag_small_wait", tok)
    big = [w_in[0], w_br_hg[0], w_br_gla[0], w_out[0], tr_(w_ff_gate), tr_(w_ff_up), w_ff_down[0]]
    big_bf = [None] + [w.astype(BF16) for w in big[1:]]
    cols = lambda g: jnp.transpose(g, (1, 0, 2)).reshape(g.shape[1], N_DEV * g.shape[2])

    def get_w_in(after):
        w_full, first = None, 0
        for s, last in enumerate(W_IN_STAGES):
            for i in range(first, last):
                land = _split_wait(w_in_pieces[i], "ag_w_in_wait%d" % i, after if w_full is None else [after, w_full],
                                   srcs=w_in_state["src"], lands=w_in_state["land"])
                w_in_state.update(src=w_in_pieces[i]["srcs"], land=land)
            rows = (first * W_IN_PIECE, last * W_IN_PIECE)
            w_in_state["land"] = [_forward_to_sibling(w_in_state["land"][0], "ag_w_in_forward%d" % s, rows)]
            w_full = _assemble_w_in(w_in_state["land"][0], rows, w_full, "assemble_w_in%d" % s)
            first = last
        return w_full

    def get_mix(after):
        g_brh, g_brg, g_out = _split_wait(mix_handle, "ag_mix_wait", after)
        return _gate_cols(cols(g_brh)), _gate_cols(cols(g_brg)), _gate_rows(g_out.reshape(D, D))

    def get_ffn(after):
        g_gate, g_up = _split_wait(ffn_handle, "ag_ffn_wait", after)

        def get_down(after):
            g_down, = _split_wait(down_handle, "ag_down_wait", after)
            return g_down.reshape(D_FF, D)

        return (g_gate.reshape(D_FF, D), g_up.reshape(D_FF, D)), get_down

    hg_lb_full = jnp.transpose(lb_g, (1, 2, 0, 3)).reshape(2, 2, HW)
    wgk_k = _layout_wgk(jnp.transpose(wgk_g, (1, 2, 0, 3)).reshape(2, 16, HW)).astype(BF16)
    bgk_k = jnp.transpose(bgk_g, (1, 0, 2)).reshape(1, D)
    onw = jnp.concatenate([jnp.tile(hg_onorm, (1, NH // 2)), jnp.tile(gla_onorm, (1, NH // 2))], axis=1)

    n_mod = w_mod.shape[2]
    a9 = jnp.concatenate([c_ctx[None], c_all[:, 0], jnp.zeros((16 - 1 - N_DEV, D), F32)], axis=0)
    b_loc = lax.dynamic_slice(b_mod, (0, me * n_mod), (1, n_mod))
    s_loc = _mod_fwd(a9, w_mod[0], b_loc)
    (mod_handle, piece), tok = _split_start([whole([s_loc]), w_in_piece(1)], "ag_mod_start", s_loc)
    started_w_in(piece)
    for i in range(2, D // W_IN_PIECE):
        (piece,), tok = _split_start([w_in_piece(i)], "ag_w_in_start%d" % i, tok)
        started_w_in(piece)
    s_all, = _split_wait(mod_handle, "ag_mod_wait", tok)
    mod_all = jnp.transpose(s_all, (1, 0, 2)).reshape(16, N_DEV * n_mod)
    pad8 = lambda m: jnp.concatenate([m.reshape(6, D), jnp.zeros((2, D), F32)], axis=0)
    modc = pad8(mod_all[0])
    modx = pad8(lax.dynamic_slice(mod_all, (1 + me, 0), (1, N_DEV * n_mod))[0])

    (mix_handle, ffn_handle, down_handle), tok = _split_start(
        [whole(big_bf[1:4]), whole(big_bf[4:6]), whole(big_bf[6:])], "ag_big_start", s_all)

    z = (ctx[0], x[0])
    modx = _tie(modx, tok, "tie_mod")
    norms = (norm_pre1, norm_post1, norm_pre2, norm_post2)
    rowshard = lambda d: d.reshape(N_DEV, d.shape[0] // N_DEV, d.shape[1]).astype(BF16)
    sent, w_in_grad = [], {}

    def w_in_chunk(i):
        half, rows = W_IN_GRAD_CHUNKS[i]
        return (_view_window(rows), w_in_grad[half], [(N_DEV, rows[1] - rows[0], D)])

    def sent_w_in(i, handle):
        w_in_grad[W_IN_GRAD_CHUNKS[i][0]] = handle["srcs"]
        sent.append(("w_in%d" % i, ["w_in#%d" % i], handle))

    def send(names, grads, x_after):
        if names == ("w_in_a",):
            w_in_grad["a"] = list(grads)
            (handle,), tok = _split_start([w_in_chunk(0)], "grads_w_in0_start", x_after)
            sent_w_in(0, handle)
            return _tie(x_after, tok, "tie_w_in0")
        if names == ("w_in_b",):
            w_in_grad["b"] = list(grads)
            return x_after
        arrs, leaves, col_arrs, col_leaves = [], [], [], []
        for nm, g in zip(names, grads):
            if nm in ("w_gate_t", "w_up_t"):
                arrs.append(rowshard(g))
                leaves.append({"w_gate_t": "w_ff_gate", "w_up_t": "w_ff_up"}[nm])
            elif nm == "w_down":
                arrs.append(rowshard(g))
                leaves.append("w_ff_down")
            elif nm == "w_out":
                arrs.append(rowshard(g[GOFF:GOFF + D]))
                leaves.append(nm)
            else:
                col_arrs.append(g[:, GOFF:GOFF + D])
                col_leaves.append(nm)
        groups = [(_view_block, arrs, [a.shape for a in arrs])]
        if col_arrs:
            groups.append((_view_cols, col_arrs, [(N_DEV, a.shape[0], D // N_DEV) for a in col_arrs]))
        handles, tok = _split_start(groups, "grads_%s_start" % names[0], x_after)
        sent.append((names[0], leaves, handles[0]))
        if col_arrs:
            sent.append((names[0] + "_cols", col_leaves, handles[1]))
        return _tie(x_after, tok, "tie_" + names[0])

    r = _local_step(z, loss_target[0], modc, modx, norms, onw, hg_lb_full, wgk_k, bgk_k,
                    get_w_in, get_mix, get_ffn, send)
    grad_x = r["grad_x"][None]

    sm_pre, sm_mid, sm_fin = r["sm_pre"], r["sm_mid"], r["sm_final"]
    dmodc = jnp.stack([sm_pre[0], sm_pre[2], sm_mid[4], sm_mid[0], sm_mid[2], sm_fin[0]]).reshape(-1)
    dmodx = jnp.stack([sm_pre[1], sm_pre[3], sm_mid[5], sm_mid[1], sm_mid[3], sm_fin[1]]).reshape(-1)
    on = r["sm_post"][0].reshape(NH, DH)
    pieces = [dmodc, dmodx, sm_pre[4], sm_mid[7], sm_mid[6], sm_fin[2], on[:NH // 2].sum(0), on[NH // 2:].sum(0),
              r["d_lb"][:2].reshape(-1), _unlayout_wgk(r["d_wgk"]).reshape(-1), r["d_bgk"][0]]
    loss_local = (0.5 / D) * jnp.sum(r["loss_vec"])
    pieces.append(jnp.concatenate([loss_local.reshape(1), jnp.zeros((DH - 1,), F32)]))
    sizes = [p.shape[0] for p in pieces]
    pack = jnp.concatenate(pieces).reshape(-1, DH)
    moms = [(m_w_in, v_w_in), (m_w_br_hg, v_w_br_hg), (m_w_br_gla, v_w_br_gla), (m_w_out, v_w_out),
            (m_w_ff_gate, v_w_ff_gate), (m_w_ff_up, v_w_ff_up), (m_w_ff_down, v_w_ff_down)]
    names = ["w_in", "w_br_hg", "w_br_gla", "w_out", "w_ff_gate", "w_ff_up", "w_ff_down"]
    wmv = {nm: (w, m, v) for nm, w, (m, v) in zip(names, big, moms)}
    res, updated = {}, {}

    def update(nm):
        w, m, v = wmv[nm]
        if nm in ("w_ff_gate", "w_ff_up"):
            outs = _adamw(recv[nm], w, tr_(m), tr_(v), "adamw_" + nm)
            res[nm] = [jnp.swapaxes(o, 0, 1)[None] for o in outs]
        else:
            outs = _adamw(recv[nm], w, m[0], v[0], "adamw_" + nm)
            res[nm] = [o[None] for o in outs]
        updated[nm] = outs[0]

    (small_handle, handle), tok = _split_start([whole([pack]), w_in_chunk(1)], "small_grads_start", pack)
    sent_w_in(1, handle)
    recv = {}
    for first, leaves, handle in sent:
        if not first.startswith("w_in"):
            recv.update(zip(leaves, _split_wait(handle, "grads_%s_wait" % first, tok)))
    update("w_ff_gate")
    update("w_ff_up")
    pack_all, = _split_wait(small_handle, "small_grads_wait", [updated["w_ff_gate"], updated["w_ff_up"]])
    tot = _sum_devices(pack_all).reshape(-1)
    offs = [sum(sizes[:i]) for i in range(len(sizes))]
    part = lambda i: tot[offs[i]:offs[i] + sizes[i]]
    dmodc_t, dmodx_t = part(0), part(1)
    g_b_mod = (dmodc_t + dmodx_t)[None]
    g_norms = [part(i)[None] for i in (2, 3, 4, 5)]
    g_hg_on, g_gla_on = part(6)[None], part(7)[None]
    lb0 = lax.dynamic_slice(part(8).reshape(2, HW), (0, me * (HW // N_DEV)), (2, HW // N_DEV))
    g_hg_lb = jnp.stack([lb0, -lb0])
    g_wgk = lax.dynamic_slice(part(9).reshape(2, 16, HW), (0, 0, me * (HW // N_DEV)), (2, 16, HW // N_DEV))[None]
    g_bgk = lax.dynamic_slice(part(10).reshape(2, HW), (0, me * (HW // N_DEV)), (2, HW // N_DEV))[None]
    loss = part(11)[0]

    dmx_all = pack_all.reshape(N_DEV, -1)[:, sizes[0]:sizes[0] + sizes[1]]
    d9 = jnp.concatenate([lax.dynamic_slice(dmodc_t[None], (0, me * n_mod), (1, n_mod)),
                          lax.dynamic_slice(dmx_all, (0, me * n_mod), (N_DEV, n_mod)),
                          jnp.zeros((16 - 1 - N_DEV, n_mod), F32)], axis=0)
    g_w_mod, dcc_part = _mod_bwd(a9, d9, w_mod[0])
    (cctx_handle, handle), tok = _split_start([whole([dcc_part]), w_in_chunk(2)], "c_ctx_start", dcc_part)
    sent_w_in(2, handle)
    recv["w_ff_down"] = _tie(recv["w_ff_down"], tok, "tie_down")
    update("w_ff_down")
    res["w_mod"] = [o[None] for o in _adamw(g_w_mod[None], w_mod[0], m_w_mod[0], v_w_mod[0], "adamw_w_mod")]
    for nm in ("w_out", "w_br_hg", "w_br_gla"):
        update(nm)
    dcc_all, = _split_wait(cctx_handle, "c_ctx_wait", [updated[nm] for nm in names[1:]] + [res["w_mod"][0]])
    g_c_ctx = _sum_devices(dcc_all)[0]

    small = [("c_ctx", c_ctx, m_c_ctx, v_c_ctx, g_c_ctx), ("b_mod", b_mod, m_b_mod, v_b_mod, g_b_mod),
             ("norm_pre1", norm_pre1, m_norm_pre1, v_norm_pre1, g_norms[0]),
             ("norm_post1", norm_post1, m_norm_post1, v_norm_post1, g_norms[1]),
             ("norm_pre2", norm_pre2, m_norm_pre2, v_norm_pre2, g_norms[2]),
             ("norm_post2", norm_post2, m_norm_post2, v_norm_post2, g_norms[3]),
             ("hg_lb", hg_lb, m_hg_lb, v_hg_lb, g_hg_lb), ("hg_onorm", hg_onorm, m_hg_onorm, v_hg_onorm, g_hg_on),
             ("gla_w_gk", gla_w_gk, m_gla_w_gk, v_gla_w_gk, g_wgk), ("gla_b_gk", gla_b_gk, m_gla_b_gk, v_gla_b_gk, g_bgk),
             ("gla_onorm", gla_onorm, m_gla_onorm, v_gla_onorm, g_gla_on)]
    flat = lambda k: jnp.concatenate([s[k].reshape(-1) for s in small]).reshape(-1, DH)
    outs = _adamw(flat(4)[None], flat(1), flat(2), flat(3), "adamw_small")
    off = 0
    for nm, w, _, _, _ in small:
        res[nm] = [o.reshape(-1)[off:off + w.size].reshape(w.shape) for o in outs]
        off += w.size

    done = [updated[nm] for nm in names[1:]] + [res["w_mod"][0]] + [o for nm, *_ in small for o in res[nm]]
    major = lambda a: jnp.transpose(a, (2, 0, 1))
    outs, row0 = None, 0
    for i, (first, leaves, handle) in enumerate(s for s in sent if s[0].startswith("w_in")):
        half = W_IN_GRAD_CHUNKS[i][0]
        land, = _split_wait(handle, "grads_%s_wait" % first, done, srcs=w_in_grad[half])
        w_in_grad[half] = handle["srcs"]
        rows = (row0, row0 + land.shape[1])
        outs = _adamw_rows3(_sum_windows(land, "sum_windows%d" % i), major(w_in), major(m_w_in), major(v_w_in),
                            "adamw_w_in%d" % i, rows, outs)
        row0 = rows[1]
    res["w_in"] = [jnp.transpose(o, (1, 2, 0)) for o in outs]

    order = ["c_ctx", "w_mod", "b_mod", "norm_pre1", "norm_post1", "norm_pre2", "norm_post2", "w_in", "hg_lb",
             "hg_onorm", "gla_w_gk", "gla_b_gk", "gla_onorm", "w_br_hg", "w_br_gla", "w_out", "w_ff_gate", "w_ff_up",
             "w_ff_down"]
    return (loss, grad_x, *[res[n][k] for k in range(4) for n in order])
```

```python
import functools

import jax
import jax.numpy as jnp
from jax import lax
from jax.experimental import pallas as pl
from jax.experimental.pallas import tpu as pltpu

F32 = jnp.float32
BF16 = jnp.bfloat16
HI = lax.Precision.HIGHEST

N_DEV = 8
D = 1024
CTX = 256
HW = 512
DH = 128
NH = 8
D_FF = 2816
EPS = 1e-6
GLA_NORM = 16.0
CHUNK = 64
TR = 256
NCT = CTX // TR
W_IN_COLS = 7168
MAIN0 = 0
LR0 = 4608
GW = 1152
GOFF = 32
GATE_HG0 = LR0
GATE_GLA0 = LR0 + D
LEVELS = (32, 16, 8)
EXP_CLAMP = 80.0
VMEM_LIMIT = 48 * 1024 * 1024

ADAM_LR, ADAM_B1, ADAM_B2, ADAM_EPS, ADAM_WD, ADAM_STEP = 0.001, 0.9, 0.999, 1e-08, 0.01, 10


def _cp(*sem):
    return pltpu.CompilerParams(dimension_semantics=sem, vmem_limit_bytes=VMEM_LIMIT)


def _sig(x):
    return jax.nn.sigmoid(x)


def _silu(x):
    return x * _sig(x)


def _dsilu(x):
    s = _sig(x)
    return s * (1.0 + x * (1.0 - s))


def _rstd(x):
    return lax.rsqrt(jnp.mean(x * x, axis=-1, keepdims=True) + EPS)


def _rms_bwd(a, y, r):
    return r * (a - y * (r * r) * jnp.mean(a * y, axis=-1, keepdims=True))


def _colsum(x):
    return jnp.sum(x, axis=0, keepdims=True)


def _dot(a, b, dims, precision=None):
    return lax.dot_general(a, b, (dims, ((), ())), preferred_element_type=F32, precision=precision)


NN = ((1,), (0,))
NT = ((1,), (1,))
TN = ((0,), (0,))

SCAN_HEADS_FWD = 4
SCAN_HEADS_BWD = 4


def _split_dot(m, x):
    mb = m.astype(BF16)
    x1 = x.astype(BF16)
    r1 = x - x1.astype(F32)
    x2 = r1.astype(BF16)
    x3 = (r1 - x2.astype(F32)).astype(BF16)
    return _dot(mb, x1, NN) + _dot(mb, x2, NN) + _dot(mb, x3, NN)


def _matmul(a, b, dims, out_dtype, name, tm, tn, tk, a_off=0, m_out=None):
    a_pair = isinstance(a, (tuple, list))
    as_ = list(a) if a_pair else [a]
    a = as_[0]
    pair = isinstance(b, (tuple, list))
    bs = list(b) if pair else [b]
    b1 = bs[0]
    rows = b1.shape[0] * len(bs)
    half = None
    if dims == NN:
        m, k, n = a.shape[0], rows, b1.shape[1]
        a_spec = pl.BlockSpec((tm, tk), lambda i, j, kk: (i, kk + a_off))
        half = b1.shape[0] // tk
        if a_pair:
            assert pair and a.shape[1] == b1.shape[0] and a_off == 0
            a_spec = [pl.BlockSpec((tm, tk), lambda i, j, kk: (i, jnp.minimum(kk, half - 1))),
                      pl.BlockSpec((tm, tk), lambda i, j, kk: (i, jnp.maximum(kk - half, 0)))]
        b_maps = [lambda i, j, kk: (kk, j)] if not pair else [
            lambda i, j, kk: (jnp.minimum(kk, half - 1), j), lambda i, j, kk: (jnp.maximum(kk - half, 0), j)]
        b_specs = [pl.BlockSpec((tk, tn), f) for f in b_maps]
        axis = 2
    elif dims == NT:
        m, k, n = a.shape[0], b1.shape[1], rows
        a_spec = pl.BlockSpec((tm, tk), lambda i, j, kk: (i, kk + a_off))
        half = b1.shape[0] // tn
        b_maps = [lambda i, j, kk: (j, kk)] if not pair else [
            lambda i, j, kk: (jnp.minimum(j, half - 1), kk), lambda i, j, kk: (jnp.maximum(j - half, 0), kk)]
        b_specs = [pl.BlockSpec((tn, tk), f) for f in b_maps]
        axis = 1
    else:
        assert not pair
        m, k = (a.shape[1] if m_out is None else m_out), a.shape[0]
        n = b1.shape[1]
        a_spec = pl.BlockSpec((tk, tm), lambda i, j, kk: (kk, i + a_off))
        b_specs = [pl.BlockSpec((tk, tn), lambda i, j, kk: (kk, j))]
    assert m % tm == 0 and n % tn == 0 and k % tk == 0, (name, m, n, k, tm, tn, tk)
    nk = k // tk
    nb = len(bs)
    na = len(as_)
    assert na == 1 or dims == NN

    def body(*refs):
        a_refs, refs = refs[:na], refs[na:]
        o_ref = refs[nb]
        if pair:
            bv = jnp.where(pl.program_id(axis) < half, refs[0][...], refs[1][...])
        else:
            bv = refs[0][...]
        av = a_refs[0][...] if na == 1 else jnp.where(pl.program_id(2) < half, a_refs[0][...], a_refs[1][...])
        part = _dot(av, bv, dims)
        if nk == 1:
            o_ref[...] = part.astype(o_ref.dtype)
            return
        acc_ref = refs[nb + 1]
        kk = pl.program_id(2)

        @pl.when(kk == 0)
        def _():
            acc_ref[...] = part

        @pl.when(kk > 0)
        def _():
            acc_ref[...] += part

        @pl.when(kk == nk - 1)
        def _():
            o_ref[...] = acc_ref[...].astype(o_ref.dtype)

    return pl.pallas_call(
        body,
        name=name,
        grid=(m // tm, n // tn, nk),
        in_specs=(a_spec if a_pair else [a_spec]) + b_specs,
        out_specs=pl.BlockSpec((tm, tn), lambda i, j, kk: (i, j)),
        out_shape=jax.ShapeDtypeStruct((m, n), out_dtype),
        scratch_shapes=[] if nk == 1 else [pltpu.VMEM((tm, tn), F32)],
        compiler_params=_cp("parallel", "parallel", "arbitrary"),
    )(*as_, *bs)


def _mm_gu_act(h, w_gate_t, w_up_t, name, tm):
    t = h.shape[0]
    tn = D_FF // 2

    def body(a_ref, bg_ref, bu_ref, u_ref, v_ref, act_ref):
        a = a_ref[...]
        u = _dot(a, bg_ref[...], NT)
        v = _dot(a, bu_ref[...], NT)
        u_ref[...] = u.astype(BF16)
        v_ref[...] = v.astype(BF16)
        act_ref[...] = (_silu(u) * v).astype(BF16)

    wspec = pl.BlockSpec((tn, D), lambda i, j: (j, 0))
    ospec = pl.BlockSpec((tm, tn), lambda i, j: (i, j))
    out = jax.ShapeDtypeStruct((t, D_FF), BF16)
    return pl.pallas_call(
        body, name=name, grid=(t // tm, D_FF // tn),
        in_specs=[pl.BlockSpec((tm, D), lambda i, j: (i, 0)), wspec, wspec],
        out_specs=[ospec] * 3, out_shape=[out] * 3,
        compiler_params=_cp("parallel", "parallel"),
    )(h, w_gate_t, w_up_t)


def _mm_down_dx_act(dy, w_down, u, v, name, tm):
    t = dy.shape[0]
    tn = D_FF // 2

    def body(a_ref, b_ref, u_ref, v_ref, du_ref, dv_ref):
        dact = _dot(a_ref[...], b_ref[...], NT)
        u = u_ref[...].astype(F32)
        du_ref[...] = (dact * v_ref[...].astype(F32) * _dsilu(u)).astype(BF16)
        dv_ref[...] = (dact * _silu(u)).astype(BF16)

    ospec = pl.BlockSpec((tm, tn), lambda i, j: (i, j))
    out = jax.ShapeDtypeStruct((t, D_FF), BF16)
    return pl.pallas_call(
        body, name=name, grid=(t // tm, D_FF // tn),
        in_specs=[pl.BlockSpec((tm, D), lambda i, j: (i, 0)), pl.BlockSpec((tn, D), lambda i, j: (j, 0)), ospec, ospec],
        out_specs=[ospec] * 2, out_shape=[out] * 2,
        compiler_params=_cp("parallel", "parallel"),
    )(dy, w_down, u, v)


def _row(c):
    return pl.BlockSpec((TR, c), lambda i: (i, 0))


def _rowcol(width, cb):
    return pl.BlockSpec((TR, width), lambda i: (i, cb))


def _full(shape):
    return pl.BlockSpec(shape, lambda i: (0,) * len(shape))


def _mod_row(mc_ref, mx_ref, k, is_ctx):
    return jnp.where(is_ctx, mc_ref[k:k + 1, :], mx_ref[k:k + 1, :])


def _z_specs():
    return [pl.BlockSpec((TR, D), lambda i: (jnp.minimum(i, NCT - 1), 0)),
            pl.BlockSpec((TR, D), lambda i: (jnp.maximum(i - NCT, 0), 0))]


def _z_tile(c_ref, x_ref, is_ctx):
    return jnp.where(is_ctx, c_ref[...], x_ref[...])


def _acc_row(ref, k, val):
    ref[k:k + 1, :] += val


def _acc_mod(ref, k, is_ctx, val):
    zero = jnp.zeros_like(val)
    ref[k:k + 1, :] += jnp.where(is_ctx, val, zero)
    ref[k + 1:k + 2, :] += jnp.where(is_ctx, zero, val)


def _prenorm(z, nw, modc, modx, i_shift, i_scale, name):
    t = z[0].shape[0] + z[1].shape[0]

    def body(zc_ref, zx_ref, nw_ref, mc_ref, mx_ref, h_ref):
        is_ctx = pl.program_id(0) < NCT
        x = _z_tile(zc_ref, zx_ref, is_ctx)
        n = x * _rstd(x) * nw_ref[...]
        h = n * (1.0 + _mod_row(mc_ref, mx_ref, i_scale, is_ctx)) + _mod_row(mc_ref, mx_ref, i_shift, is_ctx)
        h_ref[...] = h.astype(BF16)

    return pl.pallas_call(
        body, name=name, grid=(t // TR,),
        in_specs=_z_specs() + [_full((1, D)), _full((8, D)), _full((8, D))],
        out_specs=_row(D),
        out_shape=jax.ShapeDtypeStruct((t, D), BF16),
        compiler_params=_cp("parallel"),
    )(*z, nw, modc, modx)


def _hg_lb(lb_ref, d):
    a0 = lb_ref[0, d:d + 1, :]
    a1 = lb_ref[1, d:d + 1, :]
    mx = jnp.maximum(a0, a1)
    e0 = jnp.exp(a0 - mx)
    e1 = jnp.exp(a1 - mx)
    return e0 / (e0 + e1)


def _log_sigmoid(x):
    return jnp.minimum(x, 0.0) - jnp.log(1.0 + jnp.exp(-jnp.abs(x)))


def _gates_fwd(p, hg_lb, wgk, bgk):
    t = p.shape[0]
    seg = lambda j: _rowcol(HW, MAIN0 // HW + j)

    def body(hq_ref, hi_ref, hf_ref, hb_ref, gq_ref, gk_ref, gv_ref, lr_ref, lb_ref, wgk_ref, bgk_ref,
             q_ref, v_ref, kf_ref, kb_ref, gf_ref, gb_ref):
        q_ref[:, :HW] = _silu(hq_ref[...].astype(F32)).astype(BF16)
        q_ref[:, HW:] = (gq_ref[...].astype(F32) * (DH ** -0.5)).astype(BF16)
        v_ref[:, :HW] = hi_ref[...]
        v_ref[:, HW:] = gv_ref[...]
        xg = _dot(lr_ref[...].astype(BF16), wgk_ref[...], NN) + bgk_ref[...]
        for d, (raw_ref, k_ref, g_ref) in enumerate(((hf_ref, kf_ref, gf_ref), (hb_ref, kb_ref, gb_ref))):
            lbd = _hg_lb(lb_ref, d)
            f = lbd + (1.0 - lbd) * _sig(raw_ref[...].astype(F32))
            k_ref[:, :HW] = (1.0 - f).astype(BF16)
            k_ref[:, HW:] = gk_ref[...]
            g_ref[:, :HW] = jnp.log(f)
            g_ref[:, HW:] = _log_sigmoid(xg[:, d * HW:(d + 1) * HW]) * (1.0 / GLA_NORM)

    out = jax.ShapeDtypeStruct((t, D), F32)
    outb = jax.ShapeDtypeStruct((t, D), BF16)
    return pl.pallas_call(
        body, name="gates_fwd", grid=(t // TR,),
        in_specs=[seg(0), seg(1), seg(2), seg(3), seg(5), seg(6), seg(7), _rowcol(DH, LR0 // DH),
                  _full((2, 2, HW)), _full((DH, D)), _full((1, D))],
        out_specs=[_row(D)] * 6,
        out_shape=[outb] * 4 + [out] * 2,
        compiler_params=_cp("parallel"),
    )(p, p, p, p, p, p, p, p, hg_lb, wgk, bgk)


def _post_fwd(o_fw, o_bw, p, onw):
    t = o_fw.shape[0]

    def body(of_ref, ob_ref, g1_ref, g2_ref, w_ref, y_ref):
        for h in range(NH):
            sl = slice(h * DH, (h + 1) * DH)
            o = of_ref[:, sl] + ob_ref[:, sl]
            g_ref = g1_ref if h < NH // 2 else g2_ref
            gs = slice((h % (NH // 2)) * DH, (h % (NH // 2) + 1) * DH)
            n = o * _rstd(o) * w_ref[:, sl]
            y_ref[:, sl] = (n * _silu(g_ref[:, gs].astype(F32))).astype(BF16)

    return pl.pallas_call(
        body, name="post_fwd", grid=(t // TR,),
        in_specs=[_row(D), _row(D), _rowcol(HW, MAIN0 // HW + 4), _rowcol(HW, MAIN0 // HW + 8), _full((1, D))],
        out_specs=_row(D),
        out_shape=jax.ShapeDtypeStruct((t, D), BF16),
        compiler_params=_cp("parallel"),
    )(o_fw, o_bw, p, p, onw)


def _gate_window_specs(col0):
    return [_rowcol(HW, col0 // HW), _rowcol(HW, col0 // HW + 1), _rowcol(DH, (col0 + 2 * HW) // DH)]


def _gate_window(refs):
    return jnp.concatenate([r[...].astype(F32) for r in refs], axis=1)


def _branch_merge(y, w_hg, w_gla, p):
    t = y.shape[0]

    def body(y_ref, wh_ref, wg_ref, a0, a1, a2, b0, b1, b2, u1_ref, u2_ref, m_ref):
        u1 = _dot(y_ref[:, :HW], wh_ref[...], NN)
        u2 = _dot(y_ref[:, HW:], wg_ref[...], NN)
        u1_ref[...] = u1.astype(BF16)
        u2_ref[...] = u2.astype(BF16)
        m_ref[...] = (_sig(_gate_window((a0, a1, a2))) * u1 + _sig(_gate_window((b0, b1, b2))) * u2).astype(BF16)

    out = jax.ShapeDtypeStruct((t, GW), BF16)
    return pl.pallas_call(
        body, name="branch_merge", grid=(t // TR,),
        in_specs=[_row(D), _full((HW, GW)), _full((HW, GW))] + _gate_window_specs(GATE_HG0)
        + _gate_window_specs(GATE_GLA0),
        out_specs=[_row(GW)] * 3, out_shape=[out] * 3,
        compiler_params=_cp("parallel"),
    )(y, w_hg, w_gla, p, p, p, p, p, p)


def _mid_fwd(z, y1, nw_post, nw_pre, modc, modx):
    t = y1.shape[0]

    def body(zc_ref, zx_ref, y_ref, wpo_ref, wpr_ref, mc_ref, mx_ref, z1_ref, h_ref):
        is_ctx = pl.program_id(0) < NCT
        y = y_ref[...].astype(F32)
        z1 = _z_tile(zc_ref, zx_ref, is_ctx) + _mod_row(mc_ref, mx_ref, 2, is_ctx) * (y * _rstd(y) * wpo_ref[...])
        z1_ref[...] = z1
        n = z1 * _rstd(z1) * wpr_ref[...]
        h = n * (1.0 + _mod_row(mc_ref, mx_ref, 4, is_ctx)) + _mod_row(mc_ref, mx_ref, 3, is_ctx)
        h_ref[...] = h.astype(BF16)

    return pl.pallas_call(
        body, name="mid_fwd", grid=(t // TR,),
        in_specs=_z_specs() + [_row(D), _full((1, D)), _full((1, D)), _full((8, D)), _full((8, D))],
        out_specs=[_row(D), _row(D)],
        out_shape=[jax.ShapeDtypeStruct((t, D), F32), jax.ShapeDtypeStruct((t, D), BF16)],
        compiler_params=_cp("parallel"),
    )(*z, y1, nw_post, nw_pre, modc, modx)


def _final(z1, y2, target, nw, modc, modx):
    t = z1.shape[0]

    def body(z1_ref, y_ref, tg_ref, w_ref, mc_ref, mx_ref, dz_ref, dy_ref, loss_ref, sm_ref):
        i = pl.program_id(0)
        is_ctx = i < NCT

        @pl.when(i == 0)
        def _():
            loss_ref[...] = jnp.zeros_like(loss_ref)
            sm_ref[...] = jnp.zeros_like(sm_ref)

        g = _mod_row(mc_ref, mx_ref, 5, is_ctx)
        y = y_ref[...].astype(F32)
        r = _rstd(y)
        w = w_ref[...]
        yr = y * r
        n = yr * w
        e = z1_ref[...] + g * n - tg_ref[...]
        lat = jnp.where(is_ctx, 0.0, 1.0)
        loss_ref[...] += lat * _colsum(e * e)
        dz = e * (lat / D)
        dz_ref[...] = dz
        _acc_mod(sm_ref, 0, is_ctx, _colsum(dz * n))
        dn = dz * g
        _acc_row(sm_ref, 2, _colsum(dn * yr))
        dy_ref[...] = _rms_bwd(dn * w, y, r).astype(BF16)

    return pl.pallas_call(
        body, name="final", grid=(t // TR,),
        in_specs=[_row(D), _row(D), pl.BlockSpec((TR, D), lambda i: (jnp.maximum(i - NCT, 0), 0)),
                  _full((1, D)), _full((8, D)), _full((8, D))],
        out_specs=[_row(D), _row(D), _full((1, D)), _full((8, D))],
        out_shape=[jax.ShapeDtypeStruct((t, D), F32), jax.ShapeDtypeStruct((t, D), BF16),
                   jax.ShapeDtypeStruct((1, D), F32), jax.ShapeDtypeStruct((8, D), F32)],
        compiler_params=_cp("arbitrary"),
    )(z1, y2, target, nw, modc, modx)


def _mid_bwd(dh2, dz, z1, y1, nw_post, nw_pre, modc, modx):
    t = z1.shape[0]

    def body(dh_ref, dz_ref, z1_ref, y_ref, wpo_ref, wpr_ref, mc_ref, mx_ref, dzo_ref, dy_ref, sm_ref):
        i = pl.program_id(0)
        is_ctx = i < NCT

        @pl.when(i == 0)
        def _():
            sm_ref[...] = jnp.zeros_like(sm_ref)

        dh = dh_ref[...].astype(F32)
        z1 = z1_ref[...]
        r = _rstd(z1)
        zr = z1 * r
        wpr = wpr_ref[...]
        n = zr * wpr
        _acc_mod(sm_ref, 0, is_ctx, _colsum(dh))
        _acc_mod(sm_ref, 2, is_ctx, _colsum(dh * n))
        dn = dh * (1.0 + _mod_row(mc_ref, mx_ref, 4, is_ctx))
        _acc_row(sm_ref, 6, _colsum(dn * zr))
        dz1 = dz_ref[...] + _rms_bwd(dn * wpr, z1, r)
        dzo_ref[...] = dz1
        y = y_ref[...].astype(F32)
        r1 = _rstd(y)
        yr = y * r1
        wpo = wpo_ref[...]
        g = _mod_row(mc_ref, mx_ref, 2, is_ctx)
        _acc_mod(sm_ref, 4, is_ctx, _colsum(dz1 * (yr * wpo)))
        dn1 = dz1 * g
        _acc_row(sm_ref, 7, _colsum(dn1 * yr))
        dy_ref[...] = _rms_bwd(dn1 * wpo, y, r1).astype(BF16)

    return pl.pallas_call(
        body, name="mid_bwd", grid=(t // TR,),
        in_specs=[_row(D)] * 4 + [_full((1, D)), _full((1, D)), _full((8, D)), _full((8, D))],
        out_specs=[_row(D), _row(D), _full((8, D))],
        out_shape=[jax.ShapeDtypeStruct((t, D), F32), jax.ShapeDtypeStruct((t, D), BF16),
                   jax.ShapeDtypeStruct((8, D), F32)],
        compiler_params=_cp("arbitrary"),
    )(dh2, dz, z1, y1, nw_post, nw_pre, modc, modx)


def _pre_bwd(dh1, dz, z, nw, modc, modx):
    t = dh1.shape[0]

    def body(dh_ref, dz_ref, zc_ref, zx_ref, w_ref, mc_ref, mx_ref, dzo_ref, sm_ref):
        i = pl.program_id(0)
        is_ctx = i < NCT

        @pl.when(i == 0)
        def _():
            sm_ref[...] = jnp.zeros_like(sm_ref)

        dh = dh_ref[...].astype(F32)
        x = _z_tile(zc_ref, zx_ref, is_ctx)
        r = _rstd(x)
        xr = x * r
        w = w_ref[...]
        _acc_mod(sm_ref, 0, is_ctx, _colsum(dh))
        _acc_mod(sm_ref, 2, is_ctx, _colsum(dh * (xr * w)))
        dn = dh * (1.0 + _mod_row(mc_ref, mx_ref, 1, is_ctx))
        _acc_row(sm_ref, 4, _colsum(dn * xr))
        dzo_ref[...] = dz_ref[...] + _rms_bwd(dn * w, x, r)

    return pl.pallas_call(
        body, name="pre_bwd", grid=(t // TR,),
        in_specs=[_row(D)] * 2 + _z_specs() + [_full((1, D)), _full((8, D)), _full((8, D))],
        out_specs=[pl.BlockSpec((TR, D), lambda i: (jnp.maximum(i - NCT, 0), 0)), _full((8, D))],
        out_shape=[jax.ShapeDtypeStruct((t - CTX, D), F32), jax.ShapeDtypeStruct((8, D), F32)],
        compiler_params=_cp("arbitrary"),
    )(dh1, dz, *z, nw, modc, modx)


def _branch_merge_bwd(dm, p, u1, u2, w_hg, w_gla):
    t = dm.shape[0]

    def body(dm_ref, a0, a1, a2, b0, b1, b2, u1_ref, u2_ref, wh_ref, wg_ref, du1_ref, du2_ref, dg_ref, dyh_ref, dyg_ref):
        dm_ = dm_ref[...].astype(F32)
        s1 = _sig(_gate_window((a0, a1, a2)))
        s2 = _sig(_gate_window((b0, b1, b2)))
        du1 = (dm_ * s1).astype(BF16)
        du2 = (dm_ * s2).astype(BF16)
        du1_ref[...] = du1
        du2_ref[...] = du2
        dg_ref[:, :GW] = (dm_ * u1_ref[...].astype(F32) * s1 * (1.0 - s1)).astype(BF16)
        dg_ref[:, GW:] = (dm_ * u2_ref[...].astype(F32) * s2 * (1.0 - s2)).astype(BF16)
        dyh_ref[...] = _dot(du1, wh_ref[...], NT).astype(BF16)
        dyg_ref[...] = _dot(du2, wg_ref[...], NT).astype(BF16)

    return pl.pallas_call(
        body, name="branch_merge_bwd", grid=(t // TR,),
        in_specs=[_row(GW)] + _gate_window_specs(GATE_HG0) + _gate_window_specs(GATE_GLA0)
        + [_row(GW), _row(GW), _full((HW, GW)), _full((HW, GW))],
        out_specs=[_row(GW), _row(GW), _row(2 * GW), _row(HW), _row(HW)],
        out_shape=[jax.ShapeDtypeStruct((t, GW), BF16), jax.ShapeDtypeStruct((t, GW), BF16),
                   jax.ShapeDtypeStruct((t, 2 * GW), BF16), jax.ShapeDtypeStruct((t, HW), BF16),
                   jax.ShapeDtypeStruct((t, HW), BF16)],
        compiler_params=_cp("parallel"),
    )(dm, p, p, p, p, p, p, u1, u2, w_hg, w_gla)


def _post_bwd(dy_hg, dy_gla, o_fw, o_bw, p, onw):
    t = o_fw.shape[0]

    def body(d1_ref, d2_ref, of_ref, ob_ref, g1_ref, g2_ref, w_ref, do_ref, dg_ref, sm_ref):
        @pl.when(pl.program_id(0) == 0)
        def _():
            sm_ref[...] = jnp.zeros_like(sm_ref)

        for h in range(NH):
            sl = slice(h * DH, (h + 1) * DH)
            gs = slice((h % (NH // 2)) * DH, (h % (NH // 2) + 1) * DH)
            g_ref, d_ref = (g1_ref, d1_ref) if h < NH // 2 else (g2_ref, d2_ref)
            o = of_ref[:, sl] + ob_ref[:, sl]
            r = _rstd(o)
            orr = o * r
            w = w_ref[:, sl]
            gt = g_ref[:, gs].astype(F32)
            dy = d_ref[:, gs].astype(F32)
            dg_ref[:, sl] = (dy * (orr * w) * _dsilu(gt)).astype(BF16)
            dn = dy * _silu(gt)
            sm_ref[0:1, sl] += _colsum(dn * orr)
            do_ref[:, sl] = _rms_bwd(dn * w, o, r)

    return pl.pallas_call(
        body, name="post_bwd", grid=(t // TR,),
        in_specs=[_row(HW), _row(HW), _row(D), _row(D), _rowcol(HW, MAIN0 // HW + 4), _rowcol(HW, MAIN0 // HW + 8),
                  _full((1, D))],
        out_specs=[_row(D), _row(D), _full((8, D))],
        out_shape=[jax.ShapeDtypeStruct((t, D), F32), jax.ShapeDtypeStruct((t, D), BF16),
                   jax.ShapeDtypeStruct((8, D), F32)],
        compiler_params=_cp("arbitrary"),
    )(dy_hg, dy_gla, o_fw, o_bw, p, p, onw)


def _gates_bwd(p, hg_lb, wgk, bgk, dgm, dgo, dq_f, dq_b, dv_f, dv_b, dk_f, dk_b, dg_f, dg_b):
    t = p.shape[0]
    seg = lambda j: _rowcol(HW, MAIN0 // HW + j)

    def body(hq_ref, hf_ref, hb_ref, lr_ref, lb_ref, wgk_ref, bgk_ref, dgm_ref, dgo_ref,
             dqf_ref, dqb_ref, dvf_ref, dvb_ref, dkf_ref, dkb_ref, dgf_ref, dgb_ref,
             dp_ref, dlb_ref, dw_ref, db_ref):
        @pl.when(pl.program_id(0) == 0)
        def _():
            dlb_ref[...] = jnp.zeros_like(dlb_ref)
            dw_ref[...] = jnp.zeros_like(dw_ref)
            db_ref[...] = jnp.zeros_like(db_ref)

        c0 = MAIN0

        def put(j, val):
            dp_ref[:, c0 + j * HW:c0 + (j + 1) * HW] = val.astype(BF16)

        dq = dqf_ref[...].astype(F32) + dqb_ref[...].astype(F32)
        dv = dvf_ref[...].astype(F32) + dvb_ref[...].astype(F32)
        put(0, dq[:, :HW] * _dsilu(hq_ref[...].astype(F32)))
        put(1, dv[:, :HW])
        put(5, dq[:, HW:] * (DH ** -0.5))
        put(7, dv[:, HW:])
        put(6, dkf_ref[:, HW:].astype(F32) + dkb_ref[:, HW:].astype(F32))
        dp_ref[:, c0 + 4 * HW:c0 + 5 * HW] = dgo_ref[:, :HW]
        dp_ref[:, c0 + 8 * HW:c0 + 9 * HW] = dgo_ref[:, HW:]
        lr = lr_ref[...].astype(BF16)
        xg = _dot(lr, wgk_ref[...], NN) + bgk_ref[...]
        dxg = []
        for d, (raw_ref, dk_ref, dg_ref) in enumerate(((hf_ref, dkf_ref, dgf_ref), (hb_ref, dkb_ref, dgb_ref))):
            lbd = _hg_lb(lb_ref, d)
            s = _sig(raw_ref[...].astype(F32))
            f = lbd + (1.0 - lbd) * s
            df = dg_ref[:, :HW] / f - dk_ref[:, :HW].astype(F32)
            put(2 + d, df * (1.0 - lbd) * s * (1.0 - s))
            dlb_ref[d:d + 1, :] += _colsum(df * (1.0 - s)) * (lbd * (1.0 - lbd))
            dxg.append(dg_ref[:, HW:] * (1.0 / GLA_NORM) * _sig(-xg[:, d * HW:(d + 1) * HW]))
        dxg = jnp.concatenate(dxg, axis=1)
        db_ref[0:1, :] += _colsum(dxg)
        dxg_b = dxg.astype(BF16)
        dw_ref[...] += _dot(lr, dxg_b, TN)
        dlr = _dot(dxg_b, wgk_ref[...], NT)
        dp_ref[:, LR0:LR0 + DH] = (dlr + dgm_ref[:, :DH].astype(F32)).astype(BF16)
        dp_ref[:, LR0 + DH:GATE_GLA0] = dgm_ref[:, DH:D]
        dp_ref[:, GATE_GLA0:GATE_GLA0 + DH] = dgm_ref[:, D:GW] + dgm_ref[:, GW:GW + DH]
        dp_ref[:, GATE_GLA0 + DH:GATE_GLA0 + GW] = dgm_ref[:, GW + DH:]
        dp_ref[:, GATE_GLA0 + GW:] = jnp.zeros((TR, W_IN_COLS - GATE_GLA0 - GW), BF16)

    return pl.pallas_call(
        body, name="gates_bwd", grid=(t // TR,),
        in_specs=[seg(0), seg(2), seg(3), _rowcol(DH, LR0 // DH), _full((2, 2, HW)), _full((DH, D)), _full((1, D)),
                  _row(2 * GW), _row(D)] + [_row(D)] * 8,
        out_specs=[_row(W_IN_COLS), _full((8, HW)), _full((DH, D)), _full((8, D))],
        out_shape=[jax.ShapeDtypeStruct((t, W_IN_COLS), BF16), jax.ShapeDtypeStruct((8, HW), F32),
                   jax.ShapeDtypeStruct((DH, D), F32), jax.ShapeDtypeStruct((8, D), F32)],
        compiler_params=_cp("arbitrary"),
    )(p, p, p, p, hg_lb, wgk, bgk, dgm, dgo, dq_f, dq_b, dv_f, dv_b, dk_f, dk_b, dg_f, dg_b)


def _scan_consts(rev):
    r = lax.broadcasted_iota(jnp.int32, (CHUNK, CHUNK), 0)
    u = lax.broadcasted_iota(jnp.int32, (CHUNK, CHUNK), 1)
    rp = lax.broadcasted_iota(jnp.int32, (CHUNK, 1), 0)
    if rev:
        r, u, rp = CHUNK - 1 - r, CHUNK - 1 - u, CHUNK - 1 - rp
    tri = jnp.where(u <= r, 1.0, 0.0).astype(F32)
    tri_t = jnp.where(r <= u, 1.0, 0.0).astype(F32)
    lv = []
    for b in LEVELS:
        sh = b.bit_length() - 1
        pair = ((r >> sh) == (u >> sh) + 1) & (((u >> sh) & 1) == 0)
        pair_t = ((u >> sh) == (r >> sh) + 1) & (((r >> sh) & 1) == 0)
        tside = ((rp >> sh) & 1) == 1
        lv.append((pair, pair_t, tside, jnp.where(tside, 1.0, -1.0).astype(F32)))
    bd = LEVELS[-1].bit_length() - 1
    diag = ((r >> bd) == (u >> bd)) & (u <= r)
    diag_t = ((r >> bd) == (u >> bd)) & (r <= u)
    return tri, tri_t, lv, diag, diag_t


def _row_of(pos, rev):
    return CHUNK - 1 - pos if rev else pos


def _chunk_terms(cum, b_scr, consts, rev):
    _, _, lv, _, _ = consts
    terms = []
    for b, (_, _, _, sgn) in zip(LEVELS, lv):
        pieces = []
        for j in range(CHUNK // (2 * b)):
            row = _row_of(2 * b * j + b - 1, rev)
            pieces.append(jnp.broadcast_to(b_scr[row:row + 1, :], (2 * b, DH)))
        if rev:
            pieces = pieces[::-1]
        bnd = pieces[0] if len(pieces) == 1 else jnp.concatenate(pieces, axis=0)
        terms.append(jnp.exp((cum - bnd) * sgn))
    b = LEVELS[-1]
    pieces = []
    for j in range(CHUNK // b):
        if j == 0:
            pieces.append(jnp.zeros((b, DH), F32))
        else:
            row = _row_of(b * j - 1, rev)
            pieces.append(jnp.broadcast_to(b_scr[row:row + 1, :], (b, DH)))
    if rev:
        pieces = pieces[::-1]
    start = jnp.concatenate(pieces, axis=0)
    wq = jnp.exp(jnp.minimum(cum - start, 0.0))
    wk = jnp.exp(jnp.minimum(start - cum, EXP_CLAMP))
    terms.append((wq, wk))
    return terms


def _run_staged(units):
    live = list(units)
    while live:
        nxt = []
        for u in live:
            try:
                next(u)
                nxt.append(u)
            except StopIteration:
                pass
        live = nxt


SCAN_TB = 256
SCAN_CB = SCAN_TB // CHUNK


def _block_order(i, ntb, rev):
    nctx = CTX // SCAN_TB
    if not rev:
        return i
    return jnp.where(i < nctx, nctx - 1 - i, ntb - 1 - (i - nctx))


def _chunk_in_block(j, rev):
    return SCAN_CB - 1 - j if rev else j


def _scan_fwd(q, k, v, g, rev):
    t = q.shape[0]
    nc = t // CHUNK
    hpb = SCAN_HEADS_FWD

    def body(q_ref, k_ref, v_ref, g_ref, o_ref, st_ref, s_scr, b_scr):
        consts = _scan_consts(rev)
        _, _, lv, diag, _ = consts
        masks = [lvl[0] for lvl in lv] + [diag]

        @pl.when(pl.program_id(1) == 0)
        def _():
            s_scr[...] = jnp.zeros_like(s_scr)

        tri = consts[0]
        state = {hh: s_scr[hh] for hh in range(hpb)}

        def unit(hh, j):
            sl = slice(hh * DH, (hh + 1) * DH)
            c = _chunk_in_block(j, rev)
            rows = slice(c * CHUNK, (c + 1) * CHUNK)
            b_ref = b_scr.at[hh * SCAN_CB + j]
            qc, kc, vc, gc = q_ref[rows, sl], k_ref[rows, sl], v_ref[rows, sl], g_ref[rows, sl]
            cum = _split_dot(tri, gc)
            b_ref[...] = cum
            yield
            terms = _chunk_terms(cum, b_ref, consts, rev)
            qf, kf = qc.astype(F32), kc.astype(F32)
            xs = [(jnp.where(tside, qf, kf) * w).astype(BF16) for w, (_, _, tside, _) in zip(terms[:-1], lv)]
            qd, kd = (qf * terms[-1][0]).astype(BF16), (kf * terms[-1][1]).astype(BF16)
            tot = _colsum(gc)
            qe = (qf * jnp.exp(cum)).astype(BF16)
            ke = (kf * jnp.exp(tot - cum)).astype(BF16)
            vb = vc.astype(BF16)
            yield
            scs = [_dot(x, x, NT) for x in xs] + [_dot(qd, kd, NT)]
            kv = _dot(vb, ke, TN)
            yield
            a = jnp.zeros((CHUNK, CHUNK), F32)
            for sc, m in zip(scs, masks):
                a = a + jnp.where(m, sc, 0.0)
            o_intra = _dot(a.astype(BF16), vb, NN)
            yield
            st = state[hh]
            st_ref[hh, c] = st
            o_ref[rows, sl] = o_intra + _dot(qe, st.astype(BF16), NT)
            state[hh] = st * jnp.exp(tot) + kv
            yield

        _run_staged([unit(hh, j) for hh in range(hpb) for j in range(SCAN_CB)])
        for hh in range(hpb):
            s_scr[hh] = state[hh]

    ntb = t // SCAN_TB
    col = pl.BlockSpec((SCAN_TB, hpb * DH), lambda h, i: (_block_order(i, ntb, rev), h))
    return pl.pallas_call(
        body, name="scan_fwd_" + ("bw" if rev else "fw"), grid=(NH // hpb, ntb),
        in_specs=[col] * 4,
        out_specs=[col, pl.BlockSpec((hpb, SCAN_CB, DH, DH), lambda h, i: (h, _block_order(i, ntb, rev), 0, 0))],
        out_shape=[jax.ShapeDtypeStruct((t, D), F32), jax.ShapeDtypeStruct((NH, nc, DH, DH), F32)],
        scratch_shapes=[pltpu.VMEM((hpb, DH, DH), F32), pltpu.VMEM((hpb * SCAN_CB, CHUNK, DH), F32)],
        compiler_params=_cp("parallel", "arbitrary"),
    )(q, k, v, g)


def _scan_bwd(q, k, v, g, do, states, rev):
    t = q.shape[0]
    nc = t // CHUNK
    hpb = SCAN_HEADS_BWD

    def body(q_ref, k_ref, v_ref, g_ref, do_ref, st_ref, dq_ref, dk_ref, dv_ref, dg_ref, ds_scr, b_scr):
        consts = _scan_consts(rev)
        _, tri_t, lv, diag, diag_t = consts
        masks = [(lvl[0], lvl[1]) for lvl in lv] + [(diag, diag_t)]
        @pl.when(pl.program_id(1) == 0)
        def _():
            ds_scr[...] = jnp.zeros_like(ds_scr)

        tri = consts[0]
        dstate = {hh: ds_scr[hh] for hh in range(hpb)}

        def unit(hh, jj):
            sl = slice(hh * DH, (hh + 1) * DH)
            c = _chunk_in_block(SCAN_CB - 1 - jj, rev)
            rows = slice(c * CHUNK, (c + 1) * CHUNK)
            b_ref = b_scr.at[hh * SCAN_CB + jj]
            qc, kc, vc, gc = q_ref[rows, sl], k_ref[rows, sl], v_ref[rows, sl], g_ref[rows, sl]
            dob = do_ref[rows, sl].astype(BF16)
            vb = vc.astype(BF16)
            cum = _split_dot(tri, gc)
            b_ref[...] = cum
            da = _dot(dob, vb, NT)
            da_t = _dot(vb, dob, NT)
            yield
            terms = _chunk_terms(cum, b_ref, consts, rev)
            qf, kf = qc.astype(F32), kc.astype(F32)
            xs = [(jnp.where(tside, qf, kf) * w).astype(BF16) for w, (_, _, tside, _) in zip(terms[:-1], lv)]
            wqd, wkd = terms[-1]
            qdb, kdb = (qf * wqd).astype(BF16), (kf * wkd).astype(BF16)
            tot = _colsum(gc)
            e_tot = jnp.exp(tot)
            e_b = jnp.exp(cum)
            e_t = jnp.exp(tot - cum)
            qeb = (qf * e_b).astype(BF16)
            keb = (kf * e_t).astype(BF16)
            dsym = [(jnp.where(m, da, 0.0) + jnp.where(m_t, da_t, 0.0)).astype(BF16) for m, m_t in masks[:-1]]
            dad = (jnp.where(diag, da, 0.0).astype(BF16), jnp.where(diag_t, da_t, 0.0).astype(BF16))
            yield
            sym = [_dot(x, x, NT) for x in xs]
            dxs = [_dot(d, x, NN) for d, x in zip(dsym, xs)]
            at_d = _dot(kdb, qdb, NT)
            dqt_d = _dot(dad[0], kdb, NN)
            dkt_d = _dot(dad[1], qdb, NN)
            qd = _dot(dob, qeb, TN)
            yield
            a_t = jnp.where(diag_t, at_d, 0.0)
            dq = dqt_d * wqd
            dk = dkt_d * wkd
            db = dqt_d * qdb.astype(F32) - dkt_d * kdb.astype(F32)
            for s, dx, x, w, (_, m_t, tside, sgn) in zip(sym, dxs, xs, terms[:-1], lv):
                a_t = a_t + jnp.where(m_t, s, 0.0)
                dxw = dx * w
                dq = dq + jnp.where(tside, dxw, 0.0)
                dk = dk + jnp.where(tside, 0.0, dxw)
                db = db + (dx * x.astype(F32)) * sgn
            dv_intra = _dot(a_t.astype(BF16), dob, NN)
            st = st_ref[hh, c]
            stb = st.astype(BF16)
            dqe = _dot(dob, stb, NN)
            yield
            dst = dstate[hh]
            dstb = dst.astype(BF16)
            dstate[hh] = dst * e_tot + qd
            dv_ref[rows, sl] = (dv_intra + _dot(keb, dstb, NT)).astype(BF16)
            dke = _dot(vb, dstb, NN)
            yield
            qe = qeb.astype(F32)
            ke = keb.astype(F32)
            dq_ref[rows, sl] = (dq + dqe * e_b).astype(BF16)
            dk_ref[rows, sl] = (dk + dke * e_t).astype(BF16)
            db = db + dqe * qe - dke * ke
            dtot = _colsum(dstb.astype(F32) * stb.astype(F32)) * e_tot + _colsum(dke * ke)
            dg_ref[rows, sl] = _split_dot(tri_t, db) + dtot
            yield

        _run_staged([unit(hh, jj) for hh in range(hpb) for jj in range(SCAN_CB)])
        for hh in range(hpb):
            ds_scr[hh] = dstate[hh]

    ntb = t // SCAN_TB
    blk = lambda i: _block_order(ntb - 1 - i, ntb, rev)
    col = pl.BlockSpec((SCAN_TB, hpb * DH), lambda h, i: (blk(i), h))
    out = jax.ShapeDtypeStruct((t, D), F32)
    outb = jax.ShapeDtypeStruct((t, D), BF16)
    return pl.pallas_call(
        body, name="scan_bwd_" + ("bw" if rev else "fw"), grid=(NH // hpb, ntb),
        in_specs=[col] * 5 + [pl.BlockSpec((hpb, SCAN_CB, DH, DH), lambda h, i: (h, blk(i), 0, 0))],
        out_specs=[col] * 4,
        out_shape=[outb] * 3 + [out],
        scratch_shapes=[pltpu.VMEM((hpb, DH, DH), F32), pltpu.VMEM((hpb * SCAN_CB, CHUNK, DH), F32)],
        compiler_params=_cp("parallel", "arbitrary"),
    )(q, k, v, g, do, states)


W_IN_GRAD_CHUNKS = (("a", (0, 512)), ("b", (0, 256)), ("b", (256, 512)))
W_IN_REF = 6688
W_IN_PAD = 896
W_IN_PIECE = 256
W_IN_STAGES = (3, 4)


def _assemble_w_in(g, rows, prev, name):
    n, r, wp = g.shape
    tr = W_IN_PIECE
    tiles = wp // DH
    first = rows[0] // tr

    def body(g_ref, *refs):
        o_ref = refs[-1]
        lane = lax.broadcasted_iota(jnp.int32, (tr, DH), 1)
        for t in range(W_IN_COLS // DH):
            acc = None
            for j in range(n):
                c = DH * t - W_IN_SHARD * j
                if c <= -DH or c >= W_IN_SHARD:
                    continue
                k, s = divmod(c, DH)
                lo = g_ref[j, :, k * DH:(k + 1) * DH] if 0 <= k < tiles else None
                hi = g_ref[j, :, (k + 1) * DH:(k + 2) * DH] if s and 0 <= k + 1 < tiles else None
                if s:
                    zero = jnp.zeros((tr, DH), g.dtype)
                    lo = zero if lo is None else pltpu.roll(lo, DH - s, 1)
                    hi = zero if hi is None else pltpu.roll(hi, DH - s, 1)
                    part = jnp.where(lane < DH - s, lo, hi)
                else:
                    part = lo
                acc = part if acc is None else acc + part
            o_ref[:, t * DH:(t + 1) * DH] = jnp.zeros((tr, DH), g.dtype) if acc is None else acc

    held = [] if prev is None else [prev]
    return pl.pallas_call(
        body, name=name, grid=((rows[1] - rows[0]) // tr,),
        in_specs=[pl.BlockSpec((n, tr, wp), lambda i: (0, first + i, 0))] + [pl.BlockSpec(memory_space=pl.ANY)] * len(held),
        out_specs=pl.BlockSpec((tr, W_IN_COLS), lambda i: (first + i, 0)),
        out_shape=jax.ShapeDtypeStruct((r, W_IN_COLS), g.dtype),
        input_output_aliases={1: 0} if held else {},
        compiler_params=_cp("parallel"),
    )(g, *held)


def _gate_cols(w):
    return jnp.pad(w, ((0, 0), (GOFF, GW - GOFF - D)))


def _gate_rows(w):
    return jnp.pad(w, ((GOFF, GW - GOFF - D), (0, 0)))


def _layout_wgk(w):
    r = w.shape[1]
    top = jnp.concatenate([w[0], jnp.zeros_like(w[0])], axis=1)
    bot = jnp.concatenate([jnp.zeros_like(w[1]), w[1]], axis=1)
    return jnp.concatenate([top, bot, jnp.zeros((DH - 2 * r, D), w.dtype)], axis=0)


def _unlayout_wgk(d, r=16):
    return jnp.stack([d[:r, :HW], d[r:2 * r, HW:]])


def _local_step(z, target, modc, modx, norms, onw, hg_lb, wgk, bgk, get_w_in, get_mix, get_ffn, send):
    n_pre1, n_post1, n_pre2, n_post2 = norms
    t = z[0].shape[0] + z[1].shape[0]
    tm = 1152 if t % 1152 == 0 else 256
    h1 = _prenorm(z, n_pre1, modc, modx, 0, 1, "prenorm1")
    w_in = get_w_in(h1)
    p = _matmul(h1, w_in, NN, BF16, "mm_in", t, 1024, D)
    q, v, k_f, k_b, g_f, g_b = _gates_fwd(p, hg_lb, wgk, bgk)
    o_f, st_f = _scan_fwd(q, k_f, v, g_f, False)
    o_b, st_b = _scan_fwd(q, k_b, v, g_b, True)
    y = _post_fwd(o_f, o_b, p, onw)
    w_br_hg, w_br_gla, w_out = get_mix(y)
    u1, u2, merged = _branch_merge(y, w_br_hg, w_br_gla, p)
    y1 = _matmul(merged, w_out, NN, BF16, "mm_out", tm, 512, GW)
    z1, h2 = _mid_fwd(z, y1, n_post1, n_pre2, modc, modx)
    w_gu_t, get_down = get_ffn(h2)
    u, v_ff, act = _mm_gu_act(h2, w_gu_t[0], w_gu_t[1], "mm_gu", tm)
    w_down = get_down(act)
    y2 =_matmul(act, w_down, NN, BF16, "mm_down", t, 512, D_FF)
    dz, dy2, loss_vec, sm_final = _final(z1, y2, target, n_post2, modc, modx)
    du, dv_ff = _mm_down_dx_act(dy2, w_down, u, v_ff, "mm_down_dx", tm)
    d_w_down = _matmul(act, dy2, TN, BF16, "mm_down_dw", D_FF // 2, 1024, t)
    dh2 = _matmul((du, dv_ff), w_gu_t, NN, BF16, "mm_gu_dx", tm, 512, D_FF)
    d_w_gate_t = _matmul(du, h2, TN, BF16, "mm_gate_dw", D_FF // 2, 1024, t)
    d_w_up_t = _matmul(dv_ff, h2, TN, BF16, "mm_up_dw", D_FF // 2, 1024, t)
    dh2 = send(("w_down", "w_gate_t", "w_up_t"), (d_w_down, d_w_gate_t, d_w_up_t), dh2)
    dz, dy1, sm_mid = _mid_bwd(dh2, dz, z1, y1, n_post1, n_pre2, modc, modx)
    dmerged = _matmul(dy1, w_out, NT, BF16, "mm_out_dx", tm, GW, D)
    d_w_out = _matmul(merged, dy1, TN, BF16, "mm_out_dw", GW, 512, t)
    du1, du2, dgm, dy_hg, dy_gla = _branch_merge_bwd(dmerged, p, u1, u2, w_br_hg, w_br_gla)
    d_w_br_hg = _matmul(y, du1, TN, BF16, "mm_br_hg_dw", HW, GW, t, a_off=0, m_out=HW)
    d_w_br_gla = _matmul(y, du2, TN, BF16, "mm_br_gla_dw", HW, GW, t, a_off=1, m_out=HW)
    dy_hg = send(("w_out", "w_br_hg", "w_br_gla"), (d_w_out, d_w_br_hg, d_w_br_gla), dy_hg)
    do, dgo, sm_post = _post_bwd(dy_hg, dy_gla, o_f, o_b, p, onw)
    dq_f, dk_f, dv_f, dg_f = _scan_bwd(q, k_f, v, g_f, do, st_f, False)
    dq_b, dk_b, dv_b, dg_b = _scan_bwd(q, k_b, v, g_b, do, st_b, True)
    dp, d_lb, d_wgk, d_bgk = _gates_bwd(p, hg_lb, wgk, bgk, dgm, dgo, dq_f, dq_b, dv_f, dv_b, dk_f, dk_b, dg_f, dg_b)
    d_w_in_a = _matmul(h1, dp, TN, BF16, "mm_in_dw_a", 512, 1024, t, a_off=0, m_out=D // 2)
    dp = send(("w_in_a",), (d_w_in_a,), dp)
    d_w_in_b = _matmul(h1, dp, TN, BF16, "mm_in_dw_b", 512, 1024, t, a_off=1, m_out=D // 2)
    dp = send(("w_in_b",), (d_w_in_b,), dp)
    dh1 = _matmul(dp, w_in, NT, BF16, "mm_in_dx", tm, 512, W_IN_COLS // 2)
    grad_x, sm_pre = _pre_bwd(dh1, dz, z, n_pre1, modc, modx)
    return dict(loss_vec=loss_vec, grad_x=grad_x, sm_final=sm_final, sm_mid=sm_mid, sm_post=sm_post, sm_pre=sm_pre,
                d_lb=d_lb, d_wgk=d_wgk, d_bgk=d_bgk)


MESH = pl.DeviceIdType.MESH
ANY = pl.BlockSpec(memory_space=pl.ANY)
N_REL = N_DEV - 1


def _place():
    return lax.axis_index("x"), lax.axis_index("y"), lax.axis_index("c")


def _slot(p):
    return 4 * p[0] + 2 * p[1] + p[2]


HBM = pl.BlockSpec(memory_space=pltpu.HBM)
SEM = pl.BlockSpec(memory_space=pltpu.SEMAPHORE)
EFFECT = pltpu.SideEffectType.DATAFLOW_SIDE_EFFECTING


def _peer_of(x, y, c, k):
    flip = lambda v, bit: 1 - v if bit else v
    return flip(x, k & 4), flip(y, k & 2), flip(c, k & 1)


def _view_whole(src, slot):
    return src


def _view_near(src, slot):
    return src


_view_near.peers = (1, 2, 4, 6)


def _view_near_rows(rows):
    def view(src, slot):
        return src.at[pl.ds(rows[0], rows[1] - rows[0])]
    view.peers = _view_near.peers
    view.land = lambda land, slot: land.at[slot, pl.ds(rows[0], rows[1] - rows[0])]
    return view


def _view_block(src, slot):
    return src.at[slot]


def _view_cols(src, slot):
    return src.at[:, pl.ds(pl.multiple_of(slot * (D // N_DEV), D // N_DEV), D // N_DEV)]


W_IN_SHARD = W_IN_REF // N_DEV


def _view_window(rows):
    def view(src, slot):
        col0 = pl.multiple_of((W_IN_SHARD * slot // DH) * DH, DH)
        return src.at[pl.ds(rows[0], rows[1] - rows[0]), pl.ds(col0, D)]
    return view


def _split_copies(view, srcs, lands, send_sems, recv_sems, local_sems):
    x, y, c = _place()
    me = _slot((x, y, c))
    into = getattr(view, "land", lambda land, slot: land.at[slot])
    local, sends, waits = [], [], []
    for a, (src, land) in enumerate(zip(srcs, lands)):
        local.append(pltpu.make_async_copy(view(src, me), into(land, me), local_sems.at[a]))
        for k in getattr(view, "peers", range(1, N_DEV)):
            peer = _peer_of(x, y, c, k)
            mine = view(src, _slot(peer))
            sems = dict(send_sem=send_sems.at[N_REL * a + k - 1], recv_sem=recv_sems.at[N_REL * a + k - 1],
                        device_id=peer, device_id_type=MESH)
            sends.append(pltpu.make_async_remote_copy(src_ref=mine, dst_ref=into(land, me), **sems))
            waits.append(pltpu.make_async_remote_copy(src_ref=mine, dst_ref=into(land, _slot(peer)), **sems))
    return local, sends, waits


def _split_start(groups, name, after):
    built = []
    for view, srcs, lands in groups:
        lands = [lax.empty(l, s.dtype) if isinstance(l, tuple) else l for l, s in zip(lands, srcs)]
        built.append((view, list(srcs), lands))
    bufs = [b for _, srcs, lands in built for b in srcs + lands]
    nb, ng = len(bufs), len(built)

    def body(*refs):
        buf_refs, sem_refs, token = refs[:nb], refs[nb + 1:nb + 1 + 3 * ng], refs[-1]
        pos = 0
        for i, (view, srcs, _) in enumerate(built):
            n = len(srcs)
            local, sends, _ = _split_copies(view, buf_refs[pos:pos + n], buf_refs[pos + n:pos + 2 * n],
                                            *sem_refs[3 * i:3 * i + 3])
            pos += 2 * n
            for cp in local + sends:
                cp.start()
        token[...] = jnp.zeros_like(token)

    sems = []
    for _, srcs, _ in built:
        n = len(srcs)
        sems += [pltpu.SemaphoreType.DMA((N_REL * n,)), pltpu.SemaphoreType.DMA((N_REL * n,)),
                 pltpu.SemaphoreType.DMA((n,))]
    hbm = lambda a: pltpu.with_memory_space_constraint(a, pltpu.HBM)
    out = pl.pallas_call(
        body, name=name,
        out_shape=(*sems, *[pltpu.HBM(b.shape, b.dtype) for b in bufs], jax.ShapeDtypeStruct((8, DH), F32)),
        in_specs=[HBM] * nb + [ANY],
        out_specs=(*([SEM] * (3 * ng)), *([HBM] * nb), pl.BlockSpec(memory_space=pltpu.VMEM)),
        input_output_aliases={i: 3 * ng + i for i in range(nb)},
        compiler_params=pltpu.CompilerParams(has_side_effects=EFFECT),
    )(*[hbm(b) for b in bufs], after)
    handles, pos = [], 3 * ng
    for i, (view, srcs, _) in enumerate(built):
        n = len(srcs)
        handles.append(dict(view=view, n=n, sems=out[3 * i:3 * i + 3], srcs=list(out[pos:pos + n]),
                            lands=list(out[pos + n:pos + 2 * n])))
        pos += 2 * n
    return handles, out[-1]


def _split_wait(handle, name, after, srcs=None, lands=None):
    view, n, sems = handle["view"], handle["n"], handle["sems"]
    srcs = handle["srcs"] if srcs is None else srcs
    lands = handle["lands"] if lands is None else lands
    afters = list(after) if isinstance(after, (list, tuple)) else [after]

    def body(*refs):
        src_refs, land_refs = refs[:n], refs[n:2 * n]
        send_sems, recv_sems, local_sems = refs[2 * n:2 * n + 3]
        local, _, waits = _split_copies(view, src_refs, land_refs, send_sems, recv_sems, local_sems)
        for cp in waits:
            cp.wait_send()
            cp.wait_recv()
        for cp in local:
            cp.wait()

    out = pl.pallas_call(
        body, name=name,
        out_shape=(*[pltpu.HBM(s.shape, s.dtype) for s in srcs], *[pltpu.HBM(l.shape, l.dtype) for l in lands]),
        in_specs=[HBM] * (2 * n) + [SEM, SEM, SEM] + [ANY] * len(afters),
        out_specs=tuple([HBM] * (2 * n)),
        input_output_aliases={i: i for i in range(2 * n)},
        compiler_params=pltpu.CompilerParams(has_side_effects=EFFECT),
    )(*srcs, *lands, *sems, *afters)
    handle["srcs"] = list(out[:n])
    return list(out[n:])


def _tie(x, token, name):
    def body(x_ref, t_ref, o_ref):
        pass

    return pl.pallas_call(
        body, name=name, out_shape=jax.ShapeDtypeStruct(x.shape, x.dtype),
        in_specs=[ANY, ANY], out_specs=ANY, input_output_aliases={0: 0},
    )(x, token)


def _forward_to_sibling(land, name, rows):
    def body(land_ref, out_ref, send_sems, recv_sems):
        x, y, c = _place()
        sibling = (x, y, 1 - c)
        chips = [(1 - x, y), (x, 1 - y), (1 - x, 1 - y)]
        piece = pl.ds(rows[0], rows[1] - rows[0])

        def copy(j, core):
            blk = _slot((*chips[j], core))
            return pltpu.make_async_remote_copy(src_ref=land_ref.at[blk, piece], dst_ref=out_ref.at[blk, piece],
                                                send_sem=send_sems.at[j], recv_sem=recv_sems.at[j],
                                                device_id=sibling, device_id_type=MESH)

        sends = [copy(j, c) for j in range(3)]
        for cp in sends:
            cp.start()
        for j in range(3):
            copy(j, 1 - c).wait_recv()
        for cp in sends:
            cp.wait_send()

    return pl.pallas_call(
        body, name=name, in_specs=[ANY], out_specs=ANY, input_output_aliases={0: 0},
        out_shape=jax.ShapeDtypeStruct(land.shape, land.dtype),
        scratch_shapes=[pltpu.SemaphoreType.DMA((3,)), pltpu.SemaphoreType.DMA((3,))],
    )(land)


def _mod_fwd(a, w, b):
    def body(a_ref, w_ref, b_ref, o_ref):
        o_ref[...] = _dot(_silu(a_ref[...]), w_ref[...], NN, precision=HI) + b_ref[...]

    return pl.pallas_call(
        body, name="mod_fwd", out_shape=jax.ShapeDtypeStruct((a.shape[0], w.shape[1]), F32),
        compiler_params=pltpu.CompilerParams(vmem_limit_bytes=VMEM_LIMIT),
    )(a, w, b)


def _mod_bwd(a, d, w):
    def body(a_ref, d_ref, w_ref, dw_ref, dc_ref):
        av = a_ref[...]
        dv = d_ref[...]
        dw_ref[...] = _dot(_silu(av), dv, TN, precision=HI)
        da = _dot(dv[0:8, :], w_ref[...], NT, precision=HI) * _dsilu(av[0:8, :])
        row = lax.broadcasted_iota(jnp.int32, da.shape, 0)
        dc_ref[...] = jnp.where(row == 0, da, 0.0)

    return pl.pallas_call(
        body, name="mod_bwd",
        out_shape=[jax.ShapeDtypeStruct(w.shape, F32), jax.ShapeDtypeStruct((8, w.shape[0]), F32)],
        compiler_params=pltpu.CompilerParams(vmem_limit_bytes=VMEM_LIMIT),
    )(a, d, w)


def _sum_devices(g):
    def body(g_ref, o_ref):
        acc = g_ref[0]
        for i in range(1, g.shape[0]):
            acc = acc + g_ref[i]
        o_ref[...] = acc

    return pl.pallas_call(body, name="sum_devices_%d" % g.shape[1],
                          out_shape=jax.ShapeDtypeStruct(g.shape[1:], F32))(g)


def _sum_windows(g, name):
    n, r, c = g.shape
    tr = 128

    def body(g_ref, o_ref):
        x, y, cc = _place()
        lane0 = (W_IN_SHARD * _slot((x, y, cc))) % DH
        acc = g_ref[0].astype(F32)
        for i in range(1, n):
            acc = acc + g_ref[i].astype(F32)
        o_ref[...] = pltpu.roll(acc, (c - lane0) % c, 1).T

    return pl.pallas_call(
        body, name=name, grid=(r // tr,),
        in_specs=[pl.BlockSpec((n, tr, c), lambda i: (0, i, 0))],
        out_specs=pl.BlockSpec((c, tr), lambda i: (0, i)),
        out_shape=jax.ShapeDtypeStruct((c, r), F32),
        compiler_params=_cp("parallel"),
    )(g)


def _adam_rows(r, c, n):
    budget = 10 * 1024 * 1024
    best = None
    for tr in range(16, r + 1, 16):
        if r % tr == 0 and tr * c * (2 * n + 28) <= budget:
            best = tr
    return best if best is not None else r


def _adamw(g, w, m, v, name):
    n, r, c = g.shape
    tr = _adam_rows(r, c, n)
    bc1 = 1.0 - ADAM_B1 ** ADAM_STEP
    bc2 = 1.0 - ADAM_B2 ** ADAM_STEP

    def body(g_ref, w_ref, m_ref, v_ref, go_ref, d_ref, mo_ref, vo_ref):
        grad = g_ref[0].astype(F32)
        for i in range(1, n):
            grad = grad + g_ref[i].astype(F32)
        go_ref[...] = grad
        m_new = ADAM_B1 * m_ref[...] + (1.0 - ADAM_B1) * grad
        v_new = ADAM_B2 * v_ref[...] + (1.0 - ADAM_B2) * (grad * grad)
        mo_ref[...] = m_new
        vo_ref[...] = v_new
        d_ref[...] = -ADAM_LR * ((m_new / bc1) / (jnp.sqrt(v_new / bc2) + ADAM_EPS) + ADAM_WD * w_ref[...])

    blk = pl.BlockSpec((tr, c), lambda i: (i, 0))
    out = jax.ShapeDtypeStruct((r, c), F32)
    hbm = lambda a: pltpu.with_memory_space_constraint(a, pltpu.HBM)
    return pl.pallas_call(
        body, name=name, grid=(r // tr,),
        in_specs=[pl.BlockSpec((n, tr, c), lambda i: (0, i, 0)), blk, blk, blk],
        out_specs=[blk] * 4, out_shape=[out] * 4,
        compiler_params=_cp("parallel"),
    )(hbm(g), hbm(w), hbm(m), hbm(v))


ADAM_ROWS3 = 168


def _adam_math(grad, w, m, v):
    bc1 = 1.0 - ADAM_B1 ** ADAM_STEP
    bc2 = 1.0 - ADAM_B2 ** ADAM_STEP
    m_new = ADAM_B1 * m + (1.0 - ADAM_B1) * grad
    v_new = ADAM_B2 * v + (1.0 - ADAM_B2) * (grad * grad)
    delta = -ADAM_LR * ((m_new / bc1) / (jnp.sqrt(v_new / bc2) + ADAM_EPS) + ADAM_WD * w)
    return delta, m_new, v_new


def _adamw_rows3(g, w3, m3, v3, name, cols, prev):
    r, _, _ = w3.shape
    c = cols[1] - cols[0]
    n = min(-(-r // 16) * 8, ADAM_ROWS3 * D // c // 8 * 8)
    starts = list(range(0, r - n, n)) + [r - n]
    held = [] if prev is None else list(prev)

    def body(g_hbm, w_hbm, m_hbm, v_hbm, *refs):
        go_hbm, d_hbm, mo_hbm, vo_hbm, gbuf, ibuf, obuf, in_sems, out_sems = refs[len(held):]
        part = lambda h, r0: h.at[pl.ds(r0, n), 0, pl.ds(cols[0], c)]

        def fetch(p):
            r0, slot = starts[p], p % 2
            g0 = (r0 // 8) * 8
            cps = [pltpu.make_async_copy(g_hbm.at[pl.ds(g0, n + 8)], gbuf.at[slot], in_sems.at[slot, 0])]
            cps += [pltpu.make_async_copy(part(h, r0), ibuf.at[slot, k], in_sems.at[slot, 1 + k])
                    for k, h in enumerate((w_hbm, m_hbm, v_hbm))]
            for cp in cps:
                cp.start()
            return cps

        pending, outs = fetch(0), []
        for p, r0 in enumerate(starts):
            slot = p % 2
            nxt = fetch(p + 1) if p + 1 < len(starts) else []
            for cp in pending:
                cp.wait()
            grad = gbuf[slot, pl.ds(r0 - (r0 // 8) * 8, n), :]
            delta, m_new, v_new = _adam_math(grad, ibuf[slot, 0], ibuf[slot, 1], ibuf[slot, 2])
            for cp in outs:
                cp.wait()
            for k, val in enumerate((grad, delta, m_new, v_new)):
                obuf[slot, k] = val
            outs = [pltpu.make_async_copy(obuf.at[slot, k], part(h, r0), out_sems.at[slot, k])
                    for k, h in enumerate((go_hbm, d_hbm, mo_hbm, vo_hbm))]
            for cp in outs:
                cp.start()
            pending = nxt
        for cp in outs:
            cp.wait()

    out = jax.ShapeDtypeStruct(w3.shape, F32)
    return pl.pallas_call(
        body, name=name, in_specs=[ANY] * (4 + len(held)), out_specs=[ANY] * 4, out_shape=[out] * 4,
        input_output_aliases={4 + k: k for k in range(len(held))},
        scratch_shapes=[pltpu.VMEM((2, n + 8, c), F32), pltpu.VMEM((2, 3, n, c), F32), pltpu.VMEM((2, 4, n, c), F32),
                        pltpu.SemaphoreType.DMA((2, 4)), pltpu.SemaphoreType.DMA((2, 4))],
        compiler_params=pltpu.CompilerParams(vmem_limit_bytes=VMEM_LIMIT),
    )(g, w3, m3, v3, *held)


def kernel(x, c, ctx, c_ctx, w_mod, b_mod, norm_pre1, norm_post1, norm_pre2, norm_post2, w_in, hg_lb, hg_onorm, gla_w_gk, gla_b_gk, gla_onorm, w_br_hg, w_br_gla, w_out, w_ff_gate, w_ff_up, w_ff_down, loss_target, m_c_ctx, m_w_mod, m_b_mod, m_norm_pre1, m_norm_post1, m_norm_pre2, m_norm_post2, m_w_in, m_hg_lb, m_hg_onorm, m_gla_w_gk, m_gla_b_gk, m_gla_onorm, m_w_br_hg, m_w_br_gla, m_w_out, m_w_ff_gate, m_w_ff_up, m_w_ff_down, v_c_ctx, v_w_mod, v_b_mod, v_norm_pre1, v_norm_post1, v_norm_pre2, v_norm_post2, v_w_in, v_hg_lb, v_hg_onorm, v_gla_w_gk, v_gla_b_gk, v_gla_onorm, v_w_br_hg, v_w_br_gla, v_w_out, v_w_ff_gate, v_w_ff_up, v_w_ff_down):
    xi, yi, ci = lax.axis_index("x"), lax.axis_index("y"), lax.axis_index("c")
    me = 4 * xi + 2 * yi + ci
    t = CTX + x.shape[1]

    w_in_pieces, w_in_state = [], {}

    def w_in_piece(i):
        return (_view_near_rows((i * W_IN_PIECE, (i + 1) * W_IN_PIECE)), w_in_state["src"], w_in_state["land"])

    def started_w_in(handle):
        w_in_state.update(src=handle["srcs"], land=handle["lands"])
        w_in_pieces.append(handle)

    tr_ = lambda a: jnp.swapaxes(a[0], 0, 1)
    w_in_bf = jnp.pad(w_in[0].astype(BF16), ((0, 0), (0, W_IN_PAD - W_IN_SHARD)))
    w_in_state.update(src=[w_in_bf], land=[lax.empty((N_DEV,) + w_in_bf.shape, BF16)])
    gathered = lambda arrs: [(N_DEV,) + a.shape for a in arrs]
    whole = lambda arrs: (_view_whole, arrs, gathered(arrs))
    small_in = [c, hg_lb, gla_w_gk[0], gla_b_gk[0]]
    (small_handle, piece), tok = _split_start([whole(small_in), w_in_piece(0)], "ag_small_start", c)
    started_w_in(piece)
    c_all, lb_g, wgk_g, bgk_g = _split_wait(small_handle, "ag_small_wait", tok)
    big = [w_in[0], w_br_hg[0], w_br_gla[0], w_out[0], tr_(w_ff_gate), tr_(w_ff_up), w_ff_down[0]]
    big_bf = [None] + [w.astype(BF16) for w in big[1:]]
    cols = lambda g: jnp.transpose(g, (1, 0, 2)).reshape(g.shape[1], N_DEV * g.shape[2])

    def get_w_in(after):
        w_full, first = None, 0
        for s, last in enumerate(W_IN_STAGES):
            for i in range(first, last):
                land = _split_wait(w_in_pieces[i], "ag_w_in_wait%d" % i, after if w_full is None else [after, w_full],
                                   srcs=w_in_state["src"], lands=w_in_state["land"])
                w_in_state.update(src=w_in_pieces[i]["srcs"], land=land)
            rows = (first * W_IN_PIECE, last * W_IN_PIECE)
            w_in_state["land"] = [_forward_to_sibling(w_in_state["land"][0], "ag_w_in_forward%d" % s, rows)]
            w_full = _assemble_w_in(w_in_state["land"][0], rows, w_full, "assemble_w_in%d" % s)
            first = last
        return w_full

    def get_mix(after):
        g_brh, g_brg, g_out = _split_wait(mix_handle, "ag_mix_wait", after)
        return _gate_cols(cols(g_brh)), _gate_cols(cols(g_brg)), _gate_rows(g_out.reshape(D, D))

    def get_ffn(after):
        g_gate, g_up = _split_wait(ffn_handle, "ag_ffn_wait", after)

        def get_down(after):
            g_down, = _split_wait(down_handle, "ag_down_wait", after)
            return g_down.reshape(D_FF, D)

        return (g_gate.reshape(D_FF, D), g_up.reshape(D_FF, D)), get_down

    hg_lb_full = jnp.transpose(lb_g, (1, 2, 0, 3)).reshape(2, 2, HW)
    wgk_k = _layout_wgk(jnp.transpose(wgk_g, (1, 2, 0, 3)).reshape(2, 16, HW)).astype(BF16)
    bgk_k = jnp.transpose(bgk_g, (1, 0, 2)).reshape(1, D)
    onw = jnp.concatenate([jnp.tile(hg_onorm, (1, NH // 2)), jnp.tile(gla_onorm, (1, NH // 2))], axis=1)

    n_mod = w_mod.shape[2]
    a9 = jnp.concatenate([c_ctx[None], c_all[:, 0], jnp.zeros((16 - 1 - N_DEV, D), F32)], axis=0)
    b_loc = lax.dynamic_slice(b_mod, (0, me * n_mod), (1, n_mod))
    s_loc = _mod_fwd(a9, w_mod[0], b_loc)
    (mod_handle, piece), tok = _split_start([whole([s_loc]), w_in_piece(1)], "ag_mod_start", s_loc)
    started_w_in(piece)
    for i in range(2, D // W_IN_PIECE):
        (piece,), tok = _split_start([w_in_piece(i)], "ag_w_in_start%d" % i, tok)
        started_w_in(piece)
    s_all, = _split_wait(mod_handle, "ag_mod_wait", tok)
    mod_all = jnp.transpose(s_all, (1, 0, 2)).reshape(16, N_DEV * n_mod)
    pad8 = lambda m: jnp.concatenate([m.reshape(6, D), jnp.zeros((2, D), F32)], axis=0)
    modc = pad8(mod_all[0])
    modx = pad8(lax.dynamic_slice(mod_all, (1 + me, 0), (1, N_DEV * n_mod))[0])

    (mix_handle, ffn_handle, down_handle), tok = _split_start(
        [whole(big_bf[1:4]), whole(big_bf[4:6]), whole(big_bf[6:])], "ag_big_start", s_all)

    z = (ctx[0], x[0])
    modx = _tie(modx, tok, "tie_mod")
    norms = (norm_pre1, norm_post1, norm_pre2, norm_post2)
    rowshard = lambda d: d.reshape(N_DEV, d.shape[0] // N_DEV, d.shape[1]).astype(BF16)
    sent, w_in_grad = [], {}

    def w_in_chunk(i):
        half, rows = W_IN_GRAD_CHUNKS[i]
        return (_view_window(rows), w_in_grad[half], [(N_DEV, rows[1] - rows[0], D)])

    def sent_w_in(i, handle):
        w_in_grad[W_IN_GRAD_CHUNKS[i][0]] = handle["srcs"]
        sent.append(("w_in%d" % i, ["w_in#%d" % i], handle))

    def send(names, grads, x_after):
        if names == ("w_in_a",):
            w_in_grad["a"] = list(grads)
            (handle,), tok = _split_start([w_in_chunk(0)], "grads_w_in0_start", x_after)
            sent_w_in(0, handle)
            return _tie(x_after, tok, "tie_w_in0")
        if names == ("w_in_b",):
            w_in_grad["b"] = list(grads)
            return x_after
        arrs, leaves, col_arrs, col_leaves = [], [], [], []
        for nm, g in zip(names, grads):
            if nm in ("w_gate_t", "w_up_t"):
                arrs.append(rowshard(g))
                leaves.append({"w_gate_t": "w_ff_gate", "w_up_t": "w_ff_up"}[nm])
            elif nm == "w_down":
                arrs.append(rowshard(g))
                leaves.append("w_ff_down")
            elif nm == "w_out":
                arrs.append(rowshard(g[GOFF:GOFF + D]))
                leaves.append(nm)
            else:
                col_arrs.append(g[:, GOFF:GOFF + D])
                col_leaves.append(nm)
        groups = [(_view_block, arrs, [a.shape for a in arrs])]
        if col_arrs:
            groups.append((_view_cols, col_arrs, [(N_DEV, a.shape[0], D // N_DEV) for a in col_arrs]))
        handles, tok = _split_start(groups, "grads_%s_start" % names[0], x_after)
        sent.append((names[0], leaves, handles[0]))
        if col_arrs:
            sent.append((names[0] + "_cols", col_leaves, handles[1]))
        return _tie(x_after, tok, "tie_" + names[0])

    r = _local_step(z, loss_target[0], modc, modx, norms, onw, hg_lb_full, wgk_k, bgk_k,
                    get_w_in, get_mix, get_ffn, send)
    grad_x = r["grad_x"][None]

    sm_pre, sm_mid, sm_fin = r["sm_pre"], r["sm_mid"], r["sm_final"]
    dmodc = jnp.stack([sm_pre[0], sm_pre[2], sm_mid[4], sm_mid[0], sm_mid[2], sm_fin[0]]).reshape(-1)
    dmodx = jnp.stack([sm_pre[1], sm_pre[3], sm_mid[5], sm_mid[1], sm_mid[3], sm_fin[1]]).reshape(-1)
    on = r["sm_post"][0].reshape(NH, DH)
    pieces = [dmodc, dmodx, sm_pre[4], sm_mid[7], sm_mid[6], sm_fin[2], on[:NH // 2].sum(0), on[NH // 2:].sum(0),
              r["d_lb"][:2].reshape(-1), _unlayout_wgk(r["d_wgk"]).reshape(-1), r["d_bgk"][0]]
    loss_local = (0.5 / D) * jnp.sum(r["loss_vec"])
    pieces.append(jnp.concatenate([loss_local.reshape(1), jnp.zeros((DH - 1,), F32)]))
    sizes = [p.shape[0] for p in pieces]
    pack = jnp.concatenate(pieces).reshape(-1, DH)
    moms = [(m_w_in, v_w_in), (m_w_br_hg, v_w_br_hg), (m_w_br_gla, v_w_br_gla), (m_w_out, v_w_out),
            (m_w_ff_gate, v_w_ff_gate), (m_w_ff_up, v_w_ff_up), (m_w_ff_down, v_w_ff_down)]
    names = ["w_in", "w_br_hg", "w_br_gla", "w_out", "w_ff_gate", "w_ff_up", "w_ff_down"]
    wmv = {nm: (w, m, v) for nm, w, (m, v) in zip(names, big, moms)}
    res, updated = {}, {}

    def update(nm):
        w, m, v = wmv[nm]
        if nm in ("w_ff_gate", "w_ff_up"):
            outs = _adamw(recv[nm], w, tr_(m), tr_(v), "adamw_" + nm)
            res[nm] = [jnp.swapaxes(o, 0, 1)[None] for o in outs]
        else:
            outs = _adamw(recv[nm], w, m[0], v[0], "adamw_" + nm)
            res[nm] = [o[None] for o in outs]
        updated[nm] = outs[0]

    (small_handle, handle), tok = _split_start([whole([pack]), w_in_chunk(1)], "small_grads_start", pack)
    sent_w_in(1, handle)
    recv = {}
    for first, leaves, handle in sent:
        if not first.startswith("w_in"):
            recv.update(zip(leaves, _split_wait(handle, "grads_%s_wait" % first, tok)))
    update("w_ff_gate")
    update("w_ff_up")
    pack_all, = _split_wait(small_handle, "small_grads_wait", [updated["w_ff_gate"], updated["w_ff_up"]])
    tot = _sum_devices(pack_all).reshape(-1)
    offs = [sum(sizes[:i]) for i in range(len(sizes))]
    part = lambda i: tot[offs[i]:offs[i] + sizes[i]]
    dmodc_t, dmodx_t = part(0), part(1)
    g_b_mod = (dmodc_t + dmodx_t)[None]
    g_norms = [part(i)[None] for i in (2, 3, 4, 5)]
    g_hg_on, g_gla_on = part(6)[None], part(7)[None]
    lb0 = lax.dynamic_slice(part(8).reshape(2, HW), (0, me * (HW // N_DEV)), (2, HW // N_DEV))
    g_hg_lb = jnp.stack([lb0, -lb0])
    g_wgk = lax.dynamic_slice(part(9).reshape(2, 16, HW), (0, 0, me * (HW // N_DEV)), (2, 16, HW // N_DEV))[None]
    g_bgk = lax.dynamic_slice(part(10).reshape(2, HW), (0, me * (HW // N_DEV)), (2, HW // N_DEV))[None]
    loss = part(11)[0]

    dmx_all = pack_all.reshape(N_DEV, -1)[:, sizes[0]:sizes[0] + sizes[1]]
    d9 = jnp.concatenate([lax.dynamic_slice(dmodc_t[None], (0, me * n_mod), (1, n_mod)),
                          lax.dynamic_slice(dmx_all, (0, me * n_mod), (N_DEV, n_mod)),
                          jnp.zeros((16 - 1 - N_DEV, n_mod), F32)], axis=0)
    g_w_mod, dcc_part = _mod_bwd(a9, d9, w_mod[0])
    (cctx_handle, handle), tok = _split_start([whole([dcc_part]), w_in_chunk(2)], "c_ctx_start", dcc_part)
    sent_w_in(2, handle)
    recv["w_ff_down"] = _tie(recv["w_ff_down"], tok, "tie_down")
    update("w_ff_down")
    res["w_mod"] = [o[None] for o in _adamw(g_w_mod[None], w_mod[0], m_w_mod[0], v_w_mod[0], "adamw_w_mod")]
    for nm in ("w_out", "w_br_hg", "w_br_gla"):
        update(nm)
    dcc_all, = _split_wait(cctx_handle, "c_ctx_wait", [updated[nm] for nm in names[1:]] + [res["w_mod"][0]])
    g_c_ctx = _sum_devices(dcc_all)[0]

    small = [("c_ctx", c_ctx, m_c_ctx, v_c_ctx, g_c_ctx), ("b_mod", b_mod, m_b_mod, v_b_mod, g_b_mod),
             ("norm_pre1", norm_pre1, m_norm_pre1, v_norm_pre1, g_norms[0]),
             ("norm_post1", norm_post1, m_norm_post1, v_norm_post1, g_norms[1]),
             ("norm_pre2", norm_pre2, m_norm_pre2, v_norm_pre2, g_norms[2]),
             ("norm_post2", norm_post2, m_norm_post2, v_norm_post2, g_norms[3]),
             ("hg_lb", hg_lb, m_hg_lb, v_hg_lb, g_hg_lb), ("hg_onorm", hg_onorm, m_hg_onorm, v_hg_onorm, g_hg_on),
             ("gla_w_gk", gla_w_gk, m_gla_w_gk, v_gla_w_gk, g_wgk), ("gla_b_gk", gla_b_gk, m_gla_b_gk, v_gla_b_gk, g_bgk),
             ("gla_onorm", gla_onorm, m_gla_onorm, v_gla_onorm, g_gla_on)]
    flat = lambda k: jnp.concatenate([s[k].reshape(-1) for s in small]).reshape(-1, DH)
    outs = _adamw(flat(4)[None], flat(1), flat(2), flat(3), "adamw_small")
    off = 0
    for nm, w, _, _, _ in small:
        res[nm] = [o.reshape(-1)[off:off + w.size].reshape(w.shape) for o in outs]
        off += w.size

    done = [updated[nm] for nm in names[1:]] + [res["w_mod"][0]] + [o for nm, *_ in small for o in res[nm]]
    major = lambda a: jnp.transpose(a, (2, 0, 1))
    outs, row0 = None, 0
    for i, (first, leaves, handle) in enumerate(s for s in sent if s[0].startswith("w_in")):
        half = W_IN_GRAD_CHUNKS[i][0]
        land, = _split_wait(handle, "grads_%s_wait" % first, done, srcs=w_in_grad[half])
        w_in_grad[half] = handle["srcs"]
        rows = (row0, row0 + land.shape[1])
        outs = _adamw_rows3(_sum_windows(land, "sum_windows%d" % i), major(w_in), major(m_w_in), major(v_w_in),
                            "adamw_w_in%d" % i, rows, outs)
        row0 = rows[1]
    res["w_in"] = [jnp.transpose(o, (1, 2, 0)) for o in outs]

    order = ["c_ctx", "w_mod", "b_mod", "norm_pre1", "norm_post1", "norm_pre2", "norm_post2", "w_in", "hg_lb",
             "hg_onorm", "gla_w_gk", "gla_b_gk", "gla_onorm", "w_br_hg", "w_br_gla", "w_out", "w_ff_gate", "w_ff_up",
             "w_ff_down"]
    return (loss, grad_x, *[res[n][k] for k in range(4) for n in order])
```

```python
import functools

import jax
import jax.numpy as jnp
from jax import lax
from jax.experimental import pallas as pl
from jax.experimental.pallas import tpu as pltpu

F32 = jnp.float32
BF16 = jnp.bfloat16
HI = lax.Precision.HIGHEST

N_DEV = 8
D = 1024
CTX = 256
HW = 512
DH = 128
NH = 8
D_FF = 2816
EPS = 1e-6
GLA_NORM = 16.0
CHUNK = 64
TR = 256
NCT = CTX // TR
W_IN_COLS = 7168
MAIN0 = 0
LR0 = 4608
GW = 1152
GOFF = 32
GATE_HG0 = LR0
GATE_GLA0 = LR0 + D
LEVELS = (32, 16, 8)
EXP_CLAMP = 80.0
VMEM_LIMIT = 48 * 1024 * 1024

ADAM_LR, ADAM_B1, ADAM_B2, ADAM_EPS, ADAM_WD, ADAM_STEP = 0.001, 0.9, 0.999, 1e-08, 0.01, 10


def _cp(*sem):
    return pltpu.CompilerParams(dimension_semantics=sem, vmem_limit_bytes=VMEM_LIMIT)


def _sig(x):
    return jax.nn.sigmoid(x)


def _silu(x):
    return x * _sig(x)


def _dsilu(x):
    s = _sig(x)
    return s * (1.0 + x * (1.0 - s))


def _rstd(x):
    return lax.rsqrt(jnp.mean(x * x, axis=-1, keepdims=True) + EPS)


def _rms_bwd(a, y, r):
    return r * (a - y * (r * r) * jnp.mean(a * y, axis=-1, keepdims=True))


def _colsum(x):
    return jnp.sum(x, axis=0, keepdims=True)


def _dot(a, b, dims, precision=None):
    return lax.dot_general(a, b, (dims, ((), ())), preferred_element_type=F32, precision=precision)


NN = ((1,), (0,))
NT = ((1,), (1,))
TN = ((0,), (0,))

SCAN_HEADS_FWD = 4
SCAN_HEADS_BWD = 4


def _split_dot(m, x):
    mb = m.astype(BF16)
    x1 = x.astype(BF16)
    r1 = x - x1.astype(F32)
    x2 = r1.astype(BF16)
    x3 = (r1 - x2.astype(F32)).astype(BF16)
    return _dot(mb, x1, NN) + _dot(mb, x2, NN) + _dot(mb, x3, NN)


def _matmul(a, b, dims, out_dtype, name, tm, tn, tk, a_off=0, m_out=None):
    a_pair = isinstance(a, (tuple, list))
    as_ = list(a) if a_pair else [a]
    a = as_[0]
    pair = isinstance(b, (tuple, list))
    bs = list(b) if pair else [b]
    b1 = bs[0]
    rows = b1.shape[0] * len(bs)
    half = None
    if dims == NN:
        m, k, n = a.shape[0], rows, b1.shape[1]
        a_spec = pl.BlockSpec((tm, tk), lambda i, j, kk: (i, kk + a_off))
        half = b1.shape[0] // tk
        if a_pair:
            assert pair and a.shape[1] == b1.shape[0] and a_off == 0
            a_spec = [pl.BlockSpec((tm, tk), lambda i, j, kk: (i, jnp.minimum(kk, half - 1))),
                      pl.BlockSpec((tm, tk), lambda i, j, kk: (i, jnp.maximum(kk - half, 0)))]
        b_maps = [lambda i, j, kk: (kk, j)] if not pair else [
            lambda i, j, kk: (jnp.minimum(kk, half - 1), j), lambda i, j, kk: (jnp.maximum(kk - half, 0), j)]
        b_specs = [pl.BlockSpec((tk, tn), f) for f in b_maps]
        axis = 2
    elif dims == NT:
        m, k, n = a.shape[0], b1.shape[1], rows
        a_spec = pl.BlockSpec((tm, tk), lambda i, j, kk: (i, kk + a_off))
        half = b1.shape[0] // tn
        b_maps = [lambda i, j, kk: (j, kk)] if not pair else [
            lambda i, j, kk: (jnp.minimum(j, half - 1), kk), lambda i, j, kk: (jnp.maximum(j - half, 0), kk)]
        b_specs = [pl.BlockSpec((tn, tk), f) for f in b_maps]
        axis = 1
    else:
        assert not pair
        m, k = (a.shape[1] if m_out is None else m_out), a.shape[0]
        n = b1.shape[1]
        a_spec = pl.BlockSpec((tk, tm), lambda i, j, kk: (kk, i + a_off))
        b_specs = [pl.BlockSpec((tk, tn), lambda i, j, kk: (kk, j))]
    assert m % tm == 0 and n % tn == 0 and k % tk == 0, (name, m, n, k, tm, tn, tk)
    nk = k // tk
    nb = len(bs)
    na = len(as_)
    assert na == 1 or dims == NN

    def body(*refs):
        a_refs, refs = refs[:na], refs[na:]
        o_ref = refs[nb]
        if pair:
            bv = jnp.where(pl.program_id(axis) < half, refs[0][...], refs[1][...])
        else:
            bv = refs[0][...]
        av = a_refs[0][...] if na == 1 else jnp.where(pl.program_id(2) < half, a_refs[0][...], a_refs[1][...])
        part = _dot(av, bv, dims)
        if nk == 1:
            o_ref[...] = part.astype(o_ref.dtype)
            return
        acc_ref = refs[nb + 1]
        kk = pl.program_id(2)

        @pl.when(kk == 0)
        def _():
            acc_ref[...] = part

        @pl.when(kk > 0)
        def _():
            acc_ref[...] += part

        @pl.when(kk == nk - 1)
        def _():
            o_ref[...] = acc_ref[...].astype(o_ref.dtype)

    return pl.pallas_call(
        body,
        name=name,
        grid=(m // tm, n // tn, nk),
        in_specs=(a_spec if a_pair else [a_spec]) + b_specs,
        out_specs=pl.BlockSpec((tm, tn), lambda i, j, kk: (i, j)),
        out_shape=jax.ShapeDtypeStruct((m, n), out_dtype),
        scratch_shapes=[] if nk == 1 else [pltpu.VMEM((tm, tn), F32)],
        compiler_params=_cp("parallel", "parallel", "arbitrary"),
    )(*as_, *bs)


def _mm_gu_act(h, w_gate_t, w_up_t, name, tm):
    t = h.shape[0]
    tn = D_FF // 2

    def body(a_ref, bg_ref, bu_ref, u_ref, v_ref, act_ref):
        a = a_ref[...]
        u = _dot(a, bg_ref[...], NT)
        v = _dot(a, bu_ref[...], NT)
        u_ref[...] = u.astype(BF16)
        v_ref[...] = v.astype(BF16)
        act_ref[...] = (_silu(u) * v).astype(BF16)

    wspec = pl.BlockSpec((tn, D), lambda i, j: (j, 0))
    ospec = pl.BlockSpec((tm, tn), lambda i, j: (i, j))
    out = jax.ShapeDtypeStruct((t, D_FF), BF16)
    return pl.pallas_call(
        body, name=name, grid=(t // tm, D_FF // tn),
        in_specs=[pl.BlockSpec((tm, D), lambda i, j: (i, 0)), wspec, wspec],
        out_specs=[ospec] * 3, out_shape=[out] * 3,
        compiler_params=_cp("parallel", "parallel"),
    )(h, w_gate_t, w_up_t)


def _mm_down_dx_act(dy, w_down, u, v, name, tm):
    t = dy.shape[0]
    tn = D_FF // 2

    def body(a_ref, b_ref, u_ref, v_ref, du_ref, dv_ref):
        dact = _dot(a_ref[...], b_ref[...], NT)
        u = u_ref[...].astype(F32)
        du_ref[...] = (dact * v_ref[...].astype(F32) * _dsilu(u)).astype(BF16)
        dv_ref[...] = (dact * _silu(u)).astype(BF16)

    ospec = pl.BlockSpec((tm, tn), lambda i, j: (i, j))
    out = jax.ShapeDtypeStruct((t, D_FF), BF16)
    return pl.pallas_call(
        body, name=name, grid=(t // tm, D_FF // tn),
        in_specs=[pl.BlockSpec((tm, D), lambda i, j: (i, 0)), pl.BlockSpec((tn, D), lambda i, j: (j, 0)), ospec, ospec],
        out_specs=[ospec] * 2, out_shape=[out] * 2,
        compiler_params=_cp("parallel", "parallel"),
    )(dy, w_down, u, v)


def _row(c):
    return pl.BlockSpec((TR, c), lambda i: (i, 0))


def _rowcol(width, cb):
    return pl.BlockSpec((TR, width), lambda i: (i, cb))


def _full(shape):
    return pl.BlockSpec(shape, lambda i: (0,) * len(shape))


def _mod_row(mc_ref, mx_ref, k, is_ctx):
    return jnp.where(is_ctx, mc_ref[k:k + 1, :], mx_ref[k:k + 1, :])


def _z_specs():
    return [pl.BlockSpec((TR, D), lambda i: (jnp.minimum(i, NCT - 1), 0)),
            pl.BlockSpec((TR, D), lambda i: (jnp.maximum(i - NCT, 0), 0))]


def _z_tile(c_ref, x_ref, is_ctx):
    return jnp.where(is_ctx, c_ref[...], x_ref[...])


def _acc_row(ref, k, val):
    ref[k:k + 1, :] += val


def _acc_mod(ref, k, is_ctx, val):
    zero = jnp.zeros_like(val)
    ref[k:k + 1, :] += jnp.where(is_ctx, val, zero)
    ref[k + 1:k + 2, :] += jnp.where(is_ctx, zero, val)


def _prenorm(z, nw, modc, modx, i_shift, i_scale, name):
    t = z[0].shape[0] + z[1].shape[0]

    def body(zc_ref, zx_ref, nw_ref, mc_ref, mx_ref, h_ref):
        is_ctx = pl.program_id(0) < NCT
        x = _z_tile(zc_ref, zx_ref, is_ctx)
        n = x * _rstd(x) * nw_ref[...]
        h = n * (1.0 + _mod_row(mc_ref, mx_ref, i_scale, is_ctx)) + _mod_row(mc_ref, mx_ref, i_shift, is_ctx)
        h_ref[...] = h.astype(BF16)

    return pl.pallas_call(
        body, name=name, grid=(t // TR,),
        in_specs=_z_specs() + [_full((1, D)), _full((8, D)), _full((8, D))],
        out_specs=_row(D),
        out_shape=jax.ShapeDtypeStruct((t, D), BF16),
        compiler_params=_cp("parallel"),
    )(*z, nw, modc, modx)


def _hg_lb(lb_ref, d):
    a0 = lb_ref[0, d:d + 1, :]
    a1 = lb_ref[1, d:d + 1, :]
    mx = jnp.maximum(a0, a1)
    e0 = jnp.exp(a0 - mx)
    e1 = jnp.exp(a1 - mx)
    return e0 / (e0 + e1)


def _log_sigmoid(x):
    return jnp.minimum(x, 0.0) - jnp.log(1.0 + jnp.exp(-jnp.abs(x)))


def _gates_fwd(p, hg_lb, wgk, bgk):
    t = p.shape[0]
    seg = lambda j: _rowcol(HW, MAIN0 // HW + j)

    def body(hq_ref, hi_ref, hf_ref, hb_ref, gq_ref, gk_ref, gv_ref, lr_ref, lb_ref, wgk_ref, bgk_ref,
             q_ref, v_ref, kf_ref, kb_ref, gf_ref, gb_ref):
        q_ref[:, :HW] = _silu(hq_ref[...].astype(F32)).astype(BF16)
        q_ref[:, HW:] = (gq_ref[...].astype(F32) * (DH ** -0.5)).astype(BF16)
        v_ref[:, :HW] = hi_ref[...]
        v_ref[:, HW:] = gv_ref[...]
        xg = _dot(lr_ref[...].astype(BF16), wgk_ref[...], NN) + bgk_ref[...]
        for d, (raw_ref, k_ref, g_ref) in enumerate(((hf_ref, kf_ref, gf_ref), (hb_ref, kb_ref, gb_ref))):
            lbd = _hg_lb(lb_ref, d)
            f = lbd + (1.0 - lbd) * _sig(raw_ref[...].astype(F32))
            k_ref[:, :HW] = (1.0 - f).astype(BF16)
            k_ref[:, HW:] = gk_ref[...]
            g_ref[:, :HW] = jnp.log(f)
            g_ref[:, HW:] = _log_sigmoid(xg[:, d * HW:(d + 1) * HW]) * (1.0 / GLA_NORM)

    out = jax.ShapeDtypeStruct((t, D), F32)
    outb = jax.ShapeDtypeStruct((t, D), BF16)
    return pl.pallas_call(
        body, name="gates_fwd", grid=(t // TR,),
        in_specs=[seg(0), seg(1), seg(2), seg(3), seg(5), seg(6), seg(7), _rowcol(DH, LR0 // DH),
                  _full((2, 2, HW)), _full((DH, D)), _full((1, D))],
        out_specs=[_row(D)] * 6,
        out_shape=[outb] * 4 + [out] * 2,
        compiler_params=_cp("parallel"),
    )(p, p, p, p, p, p, p, p, hg_lb, wgk, bgk)


def _post_fwd(o_fw, o_bw, p, onw):
    t = o_fw.shape[0]

    def body(of_ref, ob_ref, g1_ref, g2_ref, w_ref, y_ref):
        for h in range(NH):
            sl = slice(h * DH, (h + 1) * DH)
            o = of_ref[:, sl] + ob_ref[:, sl]
            g_ref = g1_ref if h < NH // 2 else g2_ref
            gs = slice((h % (NH // 2)) * DH, (h % (NH // 2) + 1) * DH)
            n = o * _rstd(o) * w_ref[:, sl]
            y_ref[:, sl] = (n * _silu(g_ref[:, gs].astype(F32))).astype(BF16)

    return pl.pallas_call(
        body, name="post_fwd", grid=(t // TR,),
        in_specs=[_row(D), _row(D), _rowcol(HW, MAIN0 // HW + 4), _rowcol(HW, MAIN0 // HW + 8), _full((1, D))],
        out_specs=_row(D),
        out_shape=jax.ShapeDtypeStruct((t, D), BF16),
        compiler_params=_cp("parallel"),
    )(o_fw, o_bw, p, p, onw)


def _gate_window_specs(col0):
    return [_rowcol(HW, col0 // HW), _rowcol(HW, col0 // HW + 1), _rowcol(DH, (col0 + 2 * HW) // DH)]


def _gate_window(refs):
    return jnp.concatenate([r[...].astype(F32) for r in refs], axis=1)


def _branch_merge(y, w_hg, w_gla, p):
    t = y.shape[0]

    def body(y_ref, wh_ref, wg_ref, a0, a1, a2, b0, b1, b2, u1_ref, u2_ref, m_ref):
        u1 = _dot(y_ref[:, :HW], wh_ref[...], NN)
        u2 = _dot(y_ref[:, HW:], wg_ref[...], NN)
        u1_ref[...] = u1.astype(BF16)
        u2_ref[...] = u2.astype(BF16)
        m_ref[...] = (_sig(_gate_window((a0, a1, a2))) * u1 + _sig(_gate_window((b0, b1, b2))) * u2).astype(BF16)

    out = jax.ShapeDtypeStruct((t, GW), BF16)
    return pl.pallas_call(
        body, name="branch_merge", grid=(t // TR,),
        in_specs=[_row(D), _full((HW, GW)), _full((HW, GW))] + _gate_window_specs(GATE_HG0)
        + _gate_window_specs(GATE_GLA0),
        out_specs=[_row(GW)] * 3, out_shape=[out] * 3,
        compiler_params=_cp("parallel"),
    )(y, w_hg, w_gla, p, p, p, p, p, p)


def _mid_fwd(z, y1, nw_post, nw_pre, modc, modx):
    t = y1.shape[0]

    def body(zc_ref, zx_ref, y_ref, wpo_ref, wpr_ref, mc_ref, mx_ref, z1_ref, h_ref):
        is_ctx = pl.program_id(0) < NCT
        y = y_ref[...].astype(F32)
        z1 = _z_tile(zc_ref, zx_ref, is_ctx) + _mod_row(mc_ref, mx_ref, 2, is_ctx) * (y * _rstd(y) * wpo_ref[...])
        z1_ref[...] = z1
        n = z1 * _rstd(z1) * wpr_ref[...]
        h = n * (1.0 + _mod_row(mc_ref, mx_ref, 4, is_ctx)) + _mod_row(mc_ref, mx_ref, 3, is_ctx)
        h_ref[...] = h.astype(BF16)

    return pl.pallas_call(
        body, name="mid_fwd", grid=(t // TR,),
        in_specs=_z_specs() + [_row(D), _full((1, D)), _full((1, D)), _full((8, D)), _full((8, D))],
        out_specs=[_row(D), _row(D)],
        out_shape=[jax.ShapeDtypeStruct((t, D), F32), jax.ShapeDtypeStruct((t, D), BF16)],
        compiler_params=_cp("parallel"),
    )(*z, y1, nw_post, nw_pre, modc, modx)


def _final(z1, y2, target, nw, modc, modx):
    t = z1.shape[0]

    def body(z1_ref, y_ref, tg_ref, w_ref, mc_ref, mx_ref, dz_ref, dy_ref, loss_ref, sm_ref):
        i = pl.program_id(0)
        is_ctx = i < NCT

        @pl.when(i == 0)
        def _():
            loss_ref[...] = jnp.zeros_like(loss_ref)
            sm_ref[...] = jnp.zeros_like(sm_ref)

        g = _mod_row(mc_ref, mx_ref, 5, is_ctx)
        y = y_ref[...].astype(F32)
        r = _rstd(y)
        w = w_ref[...]
        yr = y * r
        n = yr * w
        e = z1_ref[...] + g * n - tg_ref[...]
        lat = jnp.where(is_ctx, 0.0, 1.0)
        loss_ref[...] += lat * _colsum(e * e)
        dz = e * (lat / D)
        dz_ref[...] = dz
        _acc_mod(sm_ref, 0, is_ctx, _colsum(dz * n))
        dn = dz * g
        _acc_row(sm_ref, 2, _colsum(dn * yr))
        dy_ref[...] = _rms_bwd(dn * w, y, r).astype(BF16)

    return pl.pallas_call(
        body, name="final", grid=(t // TR,),
        in_specs=[_row(D), _row(D), pl.BlockSpec((TR, D), lambda i: (jnp.maximum(i - NCT, 0), 0)),
                  _full((1, D)), _full((8, D)), _full((8, D))],
        out_specs=[_row(D), _row(D), _full((1, D)), _full((8, D))],
        out_shape=[jax.ShapeDtypeStruct((t, D), F32), jax.ShapeDtypeStruct((t, D), BF16),
                   jax.ShapeDtypeStruct((1, D), F32), jax.ShapeDtypeStruct((8, D), F32)],
        compiler_params=_cp("arbitrary"),
    )(z1, y2, target, nw, modc, modx)


def _mid_bwd(dh2, dz, z1, y1, nw_post, nw_pre, modc, modx):
    t = z1.shape[0]

    def body(dh_ref, dz_ref, z1_ref, y_ref, wpo_ref, wpr_ref, mc_ref, mx_ref, dzo_ref, dy_ref, sm_ref):
        i = pl.program_id(0)
        is_ctx = i < NCT

        @pl.when(i == 0)
        def _():
            sm_ref[...] = jnp.zeros_like(sm_ref)

        dh = dh_ref[...].astype(F32)
        z1 = z1_ref[...]
        r = _rstd(z1)
        zr = z1 * r
        wpr = wpr_ref[...]
        n = zr * wpr
        _acc_mod(sm_ref, 0, is_ctx, _colsum(dh))
        _acc_mod(sm_ref, 2, is_ctx, _colsum(dh * n))
        dn = dh * (1.0 + _mod_row(mc_ref, mx_ref, 4, is_ctx))
        _acc_row(sm_ref, 6, _colsum(dn * zr))
        dz1 = dz_ref[...] + _rms_bwd(dn * wpr, z1, r)
        dzo_ref[...] = dz1
        y = y_ref[...].astype(F32)
        r1 = _rstd(y)
        yr = y * r1
        wpo = wpo_ref[...]
        g = _mod_row(mc_ref, mx_ref, 2, is_ctx)
        _acc_mod(sm_ref, 4, is_ctx, _colsum(dz1 * (yr * wpo)))
        dn1 = dz1 * g
        _acc_row(sm_ref, 7, _colsum(dn1 * yr))
        dy_ref[...] = _rms_bwd(dn1 * wpo, y, r1).astype(BF16)

    return pl.pallas_call(
        body, name="mid_bwd", grid=(t // TR,),
        in_specs=[_row(D)] * 4 + [_full((1, D)), _full((1, D)), _full((8, D)), _full((8, D))],
        out_specs=[_row(D), _row(D), _full((8, D))],
        out_shape=[jax.ShapeDtypeStruct((t, D), F32), jax.ShapeDtypeStruct((t, D), BF16),
                   jax.ShapeDtypeStruct((8, D), F32)],
        compiler_params=_cp("arbitrary"),
    )(dh2, dz, z1, y1, nw_post, nw_pre, modc, modx)


def _pre_bwd(dh1, dz, z, nw, modc, modx):
    t = dh1.shape[0]

    def body(dh_ref, dz_ref, zc_ref, zx_ref, w_ref, mc_ref, mx_ref, dzo_ref, sm_ref):
        i = pl.program_id(0)
        is_ctx = i < NCT

        @pl.when(i == 0)
        def _():
            sm_ref[...] = jnp.zeros_like(sm_ref)

        dh = dh_ref[...].astype(F32)
        x = _z_tile(zc_ref, zx_ref, is_ctx)
        r = _rstd(x)
        xr = x * r
        w = w_ref[...]
        _acc_mod(sm_ref, 0, is_ctx, _colsum(dh))
        _acc_mod(sm_ref, 2, is_ctx, _colsum(dh * (xr * w)))
        dn = dh * (1.0 + _mod_row(mc_ref, mx_ref, 1, is_ctx))
        _acc_row(sm_ref, 4, _colsum(dn * xr))
        dzo_ref[...] = dz_ref[...] + _rms_bwd(dn * w, x, r)

    return pl.pallas_call(
        body, name="pre_bwd", grid=(t // TR,),
        in_specs=[_row(D)] * 2 + _z_specs() + [_full((1, D)), _full((8, D)), _full((8, D))],
        out_specs=[pl.BlockSpec((TR, D), lambda i: (jnp.maximum(i - NCT, 0), 0)), _full((8, D))],
        out_shape=[jax.ShapeDtypeStruct((t - CTX, D), F32), jax.ShapeDtypeStruct((8, D), F32)],
        compiler_params=_cp("arbitrary"),
    )(dh1, dz, *z, nw, modc, modx)


def _branch_merge_bwd(dm, p, u1, u2, w_hg, w_gla):
    t = dm.shape[0]

    def body(dm_ref, a0, a1, a2, b0, b1, b2, u1_ref, u2_ref, wh_ref, wg_ref, du1_ref, du2_ref, dg_ref, dyh_ref, dyg_ref):
        dm_ = dm_ref[...].astype(F32)
        s1 = _sig(_gate_window((a0, a1, a2)))
        s2 = _sig(_gate_window((b0, b1, b2)))
        du1 = (dm_ * s1).astype(BF16)
        du2 = (dm_ * s2).astype(BF16)
        du1_ref[...] = du1
        du2_ref[...] = du2
        dg_ref[:, :GW] = (dm_ * u1_ref[...].astype(F32) * s1 * (1.0 - s1)).astype(BF16)
        dg_ref[:, GW:] = (dm_ * u2_ref[...].astype(F32) * s2 * (1.0 - s2)).astype(BF16)
        dyh_ref[...] = _dot(du1, wh_ref[...], NT).astype(BF16)
        dyg_ref[...] = _dot(du2, wg_ref[...], NT).astype(BF16)

    return pl.pallas_call(
        body, name="branch_merge_bwd", grid=(t // TR,),
        in_specs=[_row(GW)] + _gate_window_specs(GATE_HG0) + _gate_window_specs(GATE_GLA0)
        + [_row(GW), _row(GW), _full((HW, GW)), _full((HW, GW))],
        out_specs=[_row(GW), _row(GW), _row(2 * GW), _row(HW), _row(HW)],
        out_shape=[jax.ShapeDtypeStruct((t, GW), BF16), jax.ShapeDtypeStruct((t, GW), BF16),
                   jax.ShapeDtypeStruct((t, 2 * GW), BF16), jax.ShapeDtypeStruct((t, HW), BF16),
                   jax.ShapeDtypeStruct((t, HW), BF16)],
        compiler_params=_cp("parallel"),
    )(dm, p, p, p, p, p, p, u1, u2, w_hg, w_gla)


def _post_bwd(dy_hg, dy_gla, o_fw, o_bw, p, onw):
    t = o_fw.shape[0]

    def body(d1_ref, d2_ref, of_ref, ob_ref, g1_ref, g2_ref, w_ref, do_ref, dg_ref, sm_ref):
        @pl.when(pl.program_id(0) == 0)
        def _():
            sm_ref[...] = jnp.zeros_like(sm_ref)

        for h in range(NH):
            sl = slice(h * DH, (h + 1) * DH)
            gs = slice((h % (NH // 2)) * DH, (h % (NH // 2) + 1) * DH)
            g_ref, d_ref = (g1_ref, d1_ref) if h < NH // 2 else (g2_ref, d2_ref)
            o = of_ref[:, sl] + ob_ref[:, sl]
            r = _rstd(o)
            orr = o * r
            w = w_ref[:, sl]
            gt = g_ref[:, gs].astype(F32)
            dy = d_ref[:, gs].astype(F32)
            dg_ref[:, sl] = (dy * (orr * w) * _dsilu(gt)).astype(BF16)
            dn = dy * _silu(gt)
            sm_ref[0:1, sl] += _colsum(dn * orr)
            do_ref[:, sl] = _rms_bwd(dn * w, o, r)

    return pl.pallas_call(
        body, name="post_bwd", grid=(t // TR,),
        in_specs=[_row(HW), _row(HW), _row(D), _row(D), _rowcol(HW, MAIN0 // HW + 4), _rowcol(HW, MAIN0 // HW + 8),
                  _full((1, D))],
        out_specs=[_row(D), _row(D), _full((8, D))],
        out_shape=[jax.ShapeDtypeStruct((t, D), F32), jax.ShapeDtypeStruct((t, D), BF16),
                   jax.ShapeDtypeStruct((8, D), F32)],
        compiler_params=_cp("arbitrary"),
    )(dy_hg, dy_gla, o_fw, o_bw, p, p, onw)


def _gates_bwd(p, hg_lb, wgk, bgk, dgm, dgo, dq_f, dq_b, dv_f, dv_b, dk_f, dk_b, dg_f, dg_b):
    t = p.shape[0]
    seg = lambda j: _rowcol(HW, MAIN0 // HW + j)

    def body(hq_ref, hf_ref, hb_ref, lr_ref, lb_ref, wgk_ref, bgk_ref, dgm_ref, dgo_ref,
             dqf_ref, dqb_ref, dvf_ref, dvb_ref, dkf_ref, dkb_ref, dgf_ref, dgb_ref,
             dp_ref, dlb_ref, dw_ref, db_ref):
        @pl.when(pl.program_id(0) == 0)
        def _():
            dlb_ref[...] = jnp.zeros_like(dlb_ref)
            dw_ref[...] = jnp.zeros_like(dw_ref)
            db_ref[...] = jnp.zeros_like(db_ref)

        c0 = MAIN0

        def put(j, val):
            dp_ref[:, c0 + j * HW:c0 + (j + 1) * HW] = val.astype(BF16)

        dq = dqf_ref[...].astype(F32) + dqb_ref[...].astype(F32)
        dv = dvf_ref[...].astype(F32) + dvb_ref[...].astype(F32)
        put(0, dq[:, :HW] * _dsilu(hq_ref[...].astype(F32)))
        put(1, dv[:, :HW])
        put(5, dq[:, HW:] * (DH ** -0.5))
        put(7, dv[:, HW:])
        put(6, dkf_ref[:, HW:].astype(F32) + dkb_ref[:, HW:].astype(F32))
        dp_ref[:, c0 + 4 * HW:c0 + 5 * HW] = dgo_ref[:, :HW]
        dp_ref[:, c0 + 8 * HW:c0 + 9 * HW] = dgo_ref[:, HW:]
        lr = lr_ref[...].astype(BF16)
        xg = _dot(lr, wgk_ref[...], NN) + bgk_ref[...]
        dxg = []
        for d, (raw_ref, dk_ref, dg_ref) in enumerate(((hf_ref, dkf_ref, dgf_ref), (hb_ref, dkb_ref, dgb_ref))):
            lbd = _hg_lb(lb_ref, d)
            s = _sig(raw_ref[...].astype(F32))
            f = lbd + (1.0 - lbd) * s
            df = dg_ref[:, :HW] / f - dk_ref[:, :HW].astype(F32)
            put(2 + d, df * (1.0 - lbd) * s * (1.0 - s))
            dlb_ref[d:d + 1, :] += _colsum(df * (1.0 - s)) * (lbd * (1.0 - lbd))
            dxg.append(dg_ref[:, HW:] * (1.0 / GLA_NORM) * _sig(-xg[:, d * HW:(d + 1) * HW]))
        dxg = jnp.concatenate(dxg, axis=1)
        db_ref[0:1, :] += _colsum(dxg)
        dxg_b = dxg.astype(BF16)
        dw_ref[...] += _dot(lr, dxg_b, TN)
        dlr = _dot(dxg_b, wgk_ref[...], NT)
        dp_ref[:, LR0:LR0 + DH] = (dlr + dgm_ref[:, :DH].astype(F32)).astype(BF16)
        dp_ref[:, LR0 + DH:GATE_GLA0] = dgm_ref[:, DH:D]
        dp_ref[:, GATE_GLA0:GATE_GLA0 + DH] = dgm_ref[:, D:GW] + dgm_ref[:, GW:GW + DH]
        dp_ref[:, GATE_GLA0 + DH:GATE_GLA0 + GW] = dgm_ref[:, GW + DH:]
        dp_ref[:, GATE_GLA0 + GW:] = jnp.zeros((TR, W_IN_COLS - GATE_GLA0 - GW), BF16)

    return pl.pallas_call(
        body, name="gates_bwd", grid=(t // TR,),
        in_specs=[seg(0), seg(2), seg(3), _rowcol(DH, LR0 // DH), _full((2, 2, HW)), _full((DH, D)), _full((1, D)),
                  _row(2 * GW), _row(D)] + [_row(D)] * 8,
        out_specs=[_row(W_IN_COLS), _full((8, HW)), _full((DH, D)), _full((8, D))],
        out_shape=[jax.ShapeDtypeStruct((t, W_IN_COLS), BF16), jax.ShapeDtypeStruct((8, HW), F32),
                   jax.ShapeDtypeStruct((DH, D), F32), jax.ShapeDtypeStruct((8, D), F32)],
        compiler_params=_cp("arbitrary"),
    )(p, p, p, p, hg_lb, wgk, bgk, dgm, dgo, dq_f, dq_b, dv_f, dv_b, dk_f, dk_b, dg_f, dg_b)


def _scan_consts(rev):
    r = lax.broadcasted_iota(jnp.int32, (CHUNK, CHUNK), 0)
    u = lax.broadcasted_iota(jnp.int32, (CHUNK, CHUNK), 1)
    rp = lax.broadcasted_iota(jnp.int32, (CHUNK, 1), 0)
    if rev:
        r, u, rp = CHUNK - 1 - r, CHUNK - 1 - u, CHUNK - 1 - rp
    tri = jnp.where(u <= r, 1.0, 0.0).astype(F32)
    tri_t = jnp.where(r <= u, 1.0, 0.0).astype(F32)
    lv = []
    for b in LEVELS:
        sh = b.bit_length() - 1
        pair = ((r >> sh) == (u >> sh) + 1) & (((u >> sh) & 1) == 0)
        pair_t = ((u >> sh) == (r >> sh) + 1) & (((r >> sh) & 1) == 0)
        tside = ((rp >> sh) & 1) == 1
        lv.append((pair, pair_t, tside, jnp.where(tside, 1.0, -1.0).astype(F32)))
    bd = LEVELS[-1].bit_length() - 1
    diag = ((r >> bd) == (u >> bd)) & (u <= r)
    diag_t = ((r >> bd) == (u >> bd)) & (r <= u)
    return tri, tri_t, lv, diag, diag_t


def _row_of(pos, rev):
    return CHUNK - 1 - pos if rev else pos


def _chunk_terms(cum, b_scr, consts, rev):
    _, _, lv, _, _ = consts
    terms = []
    for b, (_, _, _, sgn) in zip(LEVELS, lv):
        pieces = []
        for j in range(CHUNK // (2 * b)):
            row = _row_of(2 * b * j + b - 1, rev)
            pieces.append(jnp.broadcast_to(b_scr[row:row + 1, :], (2 * b, DH)))
        if rev:
            pieces = pieces[::-1]
        bnd = pieces[0] if len(pieces) == 1 else jnp.concatenate(pieces, axis=0)
        terms.append(jnp.exp((cum - bnd) * sgn))
    b = LEVELS[-1]
    pieces = []
    for j in range(CHUNK // b):
        if j == 0:
            pieces.append(jnp.zeros((b, DH), F32))
        else:
            row = _row_of(b * j - 1, rev)
            pieces.append(jnp.broadcast_to(b_scr[row:row + 1, :], (b, DH)))
    if rev:
        pieces = pieces[::-1]
    start = jnp.concatenate(pieces, axis=0)
    wq = jnp.exp(jnp.minimum(cum - start, 0.0))
    wk = jnp.exp(jnp.minimum(start - cum, EXP_CLAMP))
    terms.append((wq, wk))
    return terms


def _run_staged(units):
    live = list(units)
    while live:
        nxt = []
        for u in live:
            try:
                next(u)
                nxt.append(u)
            except StopIteration:
                pass
        live = nxt


SCAN_TB = 256
SCAN_CB = SCAN_TB // CHUNK


def _block_order(i, ntb, rev):
    nctx = CTX // SCAN_TB
    if not rev:
        return i
    return jnp.where(i < nctx, nctx - 1 - i, ntb - 1 - (i - nctx))


def _chunk_in_block(j, rev):
    return SCAN_CB - 1 - j if rev else j


def _scan_fwd(q, k, v, g, rev):
    t = q.shape[0]
    nc = t // CHUNK
    hpb = SCAN_HEADS_FWD

    def body(q_ref, k_ref, v_ref, g_ref, o_ref, st_ref, s_scr, b_scr):
        consts = _scan_consts(rev)
        _, _, lv, diag, _ = consts
        masks = [lvl[0] for lvl in lv] + [diag]

        @pl.when(pl.program_id(1) == 0)
        def _():
            s_scr[...] = jnp.zeros_like(s_scr)

        tri = consts[0]
        state = {hh: s_scr[hh] for hh in range(hpb)}

        def unit(hh, j):
            sl = slice(hh * DH, (hh + 1) * DH)
            c = _chunk_in_block(j, rev)
            rows = slice(c * CHUNK, (c + 1) * CHUNK)
            b_ref = b_scr.at[hh * SCAN_CB + j]
            qc, kc, vc, gc = q_ref[rows, sl], k_ref[rows, sl], v_ref[rows, sl], g_ref[rows, sl]
            cum = _split_dot(tri, gc)
            b_ref[...] = cum
            yield
            terms = _chunk_terms(cum, b_ref, consts, rev)
            qf, kf = qc.astype(F32), kc.astype(F32)
            xs = [(jnp.where(tside, qf, kf) * w).astype(BF16) for w, (_, _, tside, _) in zip(terms[:-1], lv)]
            qd, kd = (qf * terms[-1][0]).astype(BF16), (kf * terms[-1][1]).astype(BF16)
            tot = _colsum(gc)
            qe = (qf * jnp.exp(cum)).astype(BF16)
            ke = (kf * jnp.exp(tot - cum)).astype(BF16)
            vb = vc.astype(BF16)
            yield
            scs = [_dot(x, x, NT) for x in xs] + [_dot(qd, kd, NT)]
            kv = _dot(vb, ke, TN)
            yield
            a = jnp.zeros((CHUNK, CHUNK), F32)
            for sc, m in zip(scs, masks):
                a = a + jnp.where(m, sc, 0.0)
            o_intra = _dot(a.astype(BF16), vb, NN)
            yield
            st = state[hh]
            st_ref[hh, c] = st
            o_ref[rows, sl] = o_intra + _dot(qe, st.astype(BF16), NT)
            state[hh] = st * jnp.exp(tot) + kv
            yield

        _run_staged([unit(hh, j) for hh in range(hpb) for j in range(SCAN_CB)])
        for hh in range(hpb):
            s_scr[hh] = state[hh]

    ntb = t // SCAN_TB
    col = pl.BlockSpec((SCAN_TB, hpb * DH), lambda h, i: (_block_order(i, ntb, rev), h))
    return pl.pallas_call(
        body, name="scan_fwd_" + ("bw" if rev else "fw"), grid=(NH // hpb, ntb),
        in_specs=[col] * 4,
        out_specs=[col, pl.BlockSpec((hpb, SCAN_CB, DH, DH), lambda h, i: (h, _block_order(i, ntb, rev), 0, 0))],
        out_shape=[jax.ShapeDtypeStruct((t, D), F32), jax.ShapeDtypeStruct((NH, nc, DH, DH), F32)],
        scratch_shapes=[pltpu.VMEM((hpb, DH, DH), F32), pltpu.VMEM((hpb * SCAN_CB, CHUNK, DH), F32)],
        compiler_params=_cp("parallel", "arbitrary"),
    )(q, k, v, g)


def _scan_bwd(q, k, v, g, do, states, rev):
    t = q.shape[0]
    nc = t // CHUNK
    hpb = SCAN_HEADS_BWD

    def body(q_ref, k_ref, v_ref, g_ref, do_ref, st_ref, dq_ref, dk_ref, dv_ref, dg_ref, ds_scr, b_scr):
        consts = _scan_consts(rev)
        _, tri_t, lv, diag, diag_t = consts
        masks = [(lvl[0], lvl[1]) for lvl in lv] + [(diag, diag_t)]
        @pl.when(pl.program_id(1) == 0)
        def _():
            ds_scr[...] = jnp.zeros_like(ds_scr)

        tri = consts[0]
        dstate = {hh: ds_scr[hh] for hh in range(hpb)}

        def unit(hh, jj):
            sl = slice(hh * DH, (hh + 1) * DH)
            c = _chunk_in_block(SCAN_CB - 1 - jj, rev)
            rows = slice(c * CHUNK, (c + 1) * CHUNK)
            b_ref = b_scr.at[hh * SCAN_CB + jj]
            qc, kc, vc, gc = q_ref[rows, sl], k_ref[rows, sl], v_ref[rows, sl], g_ref[rows, sl]
            dob = do_ref[rows, sl].astype(BF16)
            vb = vc.astype(BF16)
            cum = _split_dot(tri, gc)
            b_ref[...] = cum
            da = _dot(dob, vb, NT)
            da_t = _dot(vb, dob, NT)
            yield
            terms = _chunk_terms(cum, b_ref, consts, rev)
            qf, kf = qc.astype(F32), kc.astype(F32)
            xs = [(jnp.where(tside, qf, kf) * w).astype(BF16) for w, (_, _, tside, _) in zip(terms[:-1], lv)]
            wqd, wkd = terms[-1]
            qdb, kdb = (qf * wqd).astype(BF16), (kf * wkd).astype(BF16)
            tot = _colsum(gc)
            e_tot = jnp.exp(tot)
            e_b = jnp.exp(cum)
            e_t = jnp.exp(tot - cum)
            qeb = (qf * e_b).astype(BF16)
            keb = (kf * e_t).astype(BF16)
            dsym = [(jnp.where(m, da, 0.0) + jnp.where(m_t, da_t, 0.0)).astype(BF16) for m, m_t in masks[:-1]]
            dad = (jnp.where(diag, da, 0.0).astype(BF16), jnp.where(diag_t, da_t, 0.0).astype(BF16))
            yield
            sym = [_dot(x, x, NT) for x in xs]
            dxs = [_dot(d, x, NN) for d, x in zip(dsym, xs)]
            at_d = _dot(kdb, qdb, NT)
            dqt_d = _dot(dad[0], kdb, NN)
            dkt_d = _dot(dad[1], qdb, NN)
            qd = _dot(dob, qeb, TN)
            yield
            a_t = jnp.where(diag_t, at_d, 0.0)
            dq = dqt_d * wqd
            dk = dkt_d * wkd
            db = dqt_d * qdb.astype(F32) - dkt_d * kdb.astype(F32)
            for s, dx, x, w, (_, m_t, tside, sgn) in zip(sym, dxs, xs, terms[:-1], lv):
                a_t = a_t + jnp.where(m_t, s, 0.0)
                dxw = dx * w
                dq = dq + jnp.where(tside, dxw, 0.0)
                dk = dk + jnp.where(tside, 0.0, dxw)
                db = db + (dx * x.astype(F32)) * sgn
            dv_intra = _dot(a_t.astype(BF16), dob, NN)
            st = st_ref[hh, c]
            stb = st.astype(BF16)
            dqe = _dot(dob, stb, NN)
            yield
            dst = dstate[hh]
            dstb = dst.astype(BF16)
            dstate[hh] = dst * e_tot + qd
            dv_ref[rows, sl] = (dv_intra + _dot(keb, dstb, NT)).astype(BF16)
            dke = _dot(vb, dstb, NN)
            yield
            qe = qeb.astype(F32)
            ke = keb.astype(F32)
            dq_ref[rows, sl] = (dq + dqe * e_b).astype(BF16)
            dk_ref[rows, sl] = (dk + dke * e_t).astype(BF16)
            db = db + dqe * qe - dke * ke
            dtot = _colsum(dstb.astype(F32) * stb.astype(F32)) * e_tot + _colsum(dke * ke)
            dg_ref[rows, sl] = _split_dot(tri_t, db) + dtot
            yield

        _run_staged([unit(hh, jj) for hh in range(hpb) for jj in range(SCAN_CB)])
        for hh in range(hpb):
            ds_scr[hh] = dstate[hh]

    ntb = t // SCAN_TB
    blk = lambda i: _block_order(ntb - 1 - i, ntb, rev)
    col = pl.BlockSpec((SCAN_TB, hpb * DH), lambda h, i: (blk(i), h))
    out = jax.ShapeDtypeStruct((t, D), F32)
    outb = jax.ShapeDtypeStruct((t, D), BF16)
    return pl.pallas_call(
        body, name="scan_bwd_" + ("bw" if rev else "fw"), grid=(NH // hpb, ntb),
        in_specs=[col] * 5 + [pl.BlockSpec((hpb, SCAN_CB, DH, DH), lambda h, i: (h, blk(i), 0, 0))],
        out_specs=[col] * 4,
        out_shape=[outb] * 3 + [out],
        scratch_shapes=[pltpu.VMEM((hpb, DH, DH), F32), pltpu.VMEM((hpb * SCAN_CB, CHUNK, DH), F32)],
        compiler_params=_cp("parallel", "arbitrary"),
    )(q, k, v, g, do, states)


W_IN_GRAD_CHUNKS = (("a", (0, 512)), ("b", (0, 384)), ("b", (384, 512)))
W_IN_REF = 6688
W_IN_PAD = 896
W_IN_PIECE = 256
W_IN_STAGES = (3, 4)


def _assemble_w_in(g, rows, prev, name):
    n, r, wp = g.shape
    tr = W_IN_PIECE
    tiles = wp // DH
    first = rows[0] // tr

    def body(g_ref, *refs):
        o_ref = refs[-1]
        lane = lax.broadcasted_iota(jnp.int32, (tr, DH), 1)
        for t in range(W_IN_COLS // DH):
            acc = None
            for j in range(n):
                c = DH * t - W_IN_SHARD * j
                if c <= -DH or c >= W_IN_SHARD:
                    continue
                k, s = divmod(c, DH)
                lo = g_ref[j, :, k * DH:(k + 1) * DH] if 0 <= k < tiles else None
                hi = g_ref[j, :, (k + 1) * DH:(k + 2) * DH] if s and 0 <= k + 1 < tiles else None
                if s:
                    zero = jnp.zeros((tr, DH), g.dtype)
                    lo = zero if lo is None else pltpu.roll(lo, DH - s, 1)
                    hi = zero if hi is None else pltpu.roll(hi, DH - s, 1)
                    part = jnp.where(lane < DH - s, lo, hi)
                else:
                    part = lo
                acc = part if acc is None else acc + part
            o_ref[:, t * DH:(t + 1) * DH] = jnp.zeros((tr, DH), g.dtype) if acc is None else acc

    held = [] if prev is None else [prev]
    return pl.pallas_call(
        body, name=name, grid=((rows[1] - rows[0]) // tr,),
        in_specs=[pl.BlockSpec((n, tr, wp), lambda i: (0, first + i, 0))] + [pl.BlockSpec(memory_space=pl.ANY)] * len(held),
        out_specs=pl.BlockSpec((tr, W_IN_COLS), lambda i: (first + i, 0)),
        out_shape=jax.ShapeDtypeStruct((r, W_IN_COLS), g.dtype),
        input_output_aliases={1: 0} if held else {},
        compiler_params=_cp("parallel"),
    )(g, *held)


def _gate_cols(w):
    return jnp.pad(w, ((0, 0), (GOFF, GW - GOFF - D)))


def _gate_rows(w):
    return jnp.pad(w, ((GOFF, GW - GOFF - D), (0, 0)))


def _layout_wgk(w):
    r = w.shape[1]
    top = jnp.concatenate([w[0], jnp.zeros_like(w[0])], axis=1)
    bot = jnp.concatenate([jnp.zeros_like(w[1]), w[1]], axis=1)
    return jnp.concatenate([top, bot, jnp.zeros((DH - 2 * r, D), w.dtype)], axis=0)


def _unlayout_wgk(d, r=16):
    return jnp.stack([d[:r, :HW], d[r:2 * r, HW:]])


def _local_step(z, target, modc, modx, norms, onw, hg_lb, wgk, bgk, get_w_in, get_mix, get_ffn, send):
    n_pre1, n_post1, n_pre2, n_post2 = norms
    t = z[0].shape[0] + z[1].shape[0]
    tm = 1152 if t % 1152 == 0 else 256
    h1 = _prenorm(z, n_pre1, modc, modx, 0, 1, "prenorm1")
    w_in = get_w_in(h1)
    p = _matmul(h1, w_in, NN, BF16, "mm_in", t, 1024, D)
    q, v, k_f, k_b, g_f, g_b = _gates_fwd(p, hg_lb, wgk, bgk)
    o_f, st_f = _scan_fwd(q, k_f, v, g_f, False)
    o_b, st_b = _scan_fwd(q, k_b, v, g_b, True)
    y = _post_fwd(o_f, o_b, p, onw)
    w_br_hg, w_br_gla, w_out = get_mix(y)
    u1, u2, merged = _branch_merge(y, w_br_hg, w_br_gla, p)
    y1 = _matmul(merged, w_out, NN, BF16, "mm_out", tm, 512, GW)
    z1, h2 = _mid_fwd(z, y1, n_post1, n_pre2, modc, modx)
    w_gu_t, get_down = get_ffn(h2)
    u, v_ff, act = _mm_gu_act(h2, w_gu_t[0], w_gu_t[1], "mm_gu", tm)
    w_down = get_down(act)
    y2 =_matmul(act, w_down, NN, BF16, "mm_down", t, 512, D_FF)
    dz, dy2, loss_vec, sm_final = _final(z1, y2, target, n_post2, modc, modx)
    du, dv_ff = _mm_down_dx_act(dy2, w_down, u, v_ff, "mm_down_dx", tm)
    d_w_down = _matmul(act, dy2, TN, BF16, "mm_down_dw", D_FF // 2, 1024, t)
    dh2 = _matmul((du, dv_ff), w_gu_t, NN, BF16, "mm_gu_dx", tm, 512, D_FF)
    d_w_gate_t = _matmul(du, h2, TN, BF16, "mm_gate_dw", D_FF // 2, 1024, t)
    d_w_up_t = _matmul(dv_ff, h2, TN, BF16, "mm_up_dw", D_FF // 2, 1024, t)
    dh2 = send(("w_down", "w_gate_t", "w_up_t"), (d_w_down, d_w_gate_t, d_w_up_t), dh2)
    dz, dy1, sm_mid = _mid_bwd(dh2, dz, z1, y1, n_post1, n_pre2, modc, modx)
    dmerged = _matmul(dy1, w_out, NT, BF16, "mm_out_dx", tm, GW, D)
    d_w_out = _matmul(merged, dy1, TN, BF16, "mm_out_dw", GW, 512, t)
    du1, du2, dgm, dy_hg, dy_gla = _branch_merge_bwd(dmerged, p, u1, u2, w_br_hg, w_br_gla)
    d_w_br_hg = _matmul(y, du1, TN, BF16, "mm_br_hg_dw", HW, GW, t, a_off=0, m_out=HW)
    d_w_br_gla = _matmul(y, du2, TN, BF16, "mm_br_gla_dw", HW, GW, t, a_off=1, m_out=HW)
    dy_hg = send(("w_out", "w_br_hg", "w_br_gla"), (d_w_out, d_w_br_hg, d_w_br_gla), dy_hg)
    do, dgo, sm_post = _post_bwd(dy_hg, dy_gla, o_f, o_b, p, onw)
    dq_f, dk_f, dv_f, dg_f = _scan_bwd(q, k_f, v, g_f, do, st_f, False)
    dq_b, dk_b, dv_b, dg_b = _scan_bwd(q, k_b, v, g_b, do, st_b, True)
    dp, d_lb, d_wgk, d_bgk = _gates_bwd(p, hg_lb, wgk, bgk, dgm, dgo, dq_f, dq_b, dv_f, dv_b, dk_f, dk_b, dg_f, dg_b)
    d_w_in_a = _matmul(h1, dp, TN, BF16, "mm_in_dw_a", 512, 1024, t, a_off=0, m_out=D // 2)
    dp = send(("w_in_a",), (d_w_in_a,), dp)
    d_w_in_b = _matmul(h1, dp, TN, BF16, "mm_in_dw_b", 512, 1024, t, a_off=1, m_out=D // 2)
    dp = send(("w_in_b",), (d_w_in_b,), dp)
    dh1 = _matmul(dp, w_in, NT, BF16, "mm_in_dx", tm, 512, W_IN_COLS // 2)
    grad_x, sm_pre = _pre_bwd(dh1, dz, z, n_pre1, modc, modx)
    return dict(loss_vec=loss_vec, grad_x=grad_x, sm_final=sm_final, sm_mid=sm_mid, sm_post=sm_post, sm_pre=sm_pre,
                d_lb=d_lb, d_wgk=d_wgk, d_bgk=d_bgk)


MESH = pl.DeviceIdType.MESH
ANY = pl.BlockSpec(memory_space=pl.ANY)
N_REL = N_DEV - 1


def _place():
    return lax.axis_index("x"), lax.axis_index("y"), lax.axis_index("c")


def _slot(p):
    return 4 * p[0] + 2 * p[1] + p[2]


HBM = pl.BlockSpec(memory_space=pltpu.HBM)
SEM = pl.BlockSpec(memory_space=pltpu.SEMAPHORE)
EFFECT = pltpu.SideEffectType.DATAFLOW_SIDE_EFFECTING


def _peer_of(x, y, c, k):
    flip = lambda v, bit: 1 - v if bit else v
    return flip(x, k & 4), flip(y, k & 2), flip(c, k & 1)


def _view_whole(src, slot):
    return src


def _view_near(src, slot):
    return src


_view_near.peers = (1, 2, 4, 6)


def _view_near_rows(rows):
    def view(src, slot):
        return src.at[pl.ds(rows[0], rows[1] - rows[0])]
    view.peers = _view_near.peers
    view.land = lambda land, slot: land.at[slot, pl.ds(rows[0], rows[1] - rows[0])]
    return view


def _view_block(src, slot):
    return src.at[slot]


def _view_cols(src, slot):
    return src.at[:, pl.ds(pl.multiple_of(slot * (D // N_DEV), D // N_DEV), D // N_DEV)]


W_IN_SHARD = W_IN_REF // N_DEV


def _view_window(rows):
    def view(src, slot):
        col0 = pl.multiple_of((W_IN_SHARD * slot // DH) * DH, DH)
        return src.at[pl.ds(rows[0], rows[1] - rows[0]), pl.ds(col0, D)]
    return view


def _split_copies(view, srcs, lands, send_sems, recv_sems, local_sems):
    x, y, c = _place()
    me = _slot((x, y, c))
    into = getattr(view, "land", lambda land, slot: land.at[slot])
    local, sends, waits = [], [], []
    for a, (src, land) in enumerate(zip(srcs, lands)):
        local.append(pltpu.make_async_copy(view(src, me), into(land, me), local_sems.at[a]))
        for k in getattr(view, "peers", range(1, N_DEV)):
            peer = _peer_of(x, y, c, k)
            mine = view(src, _slot(peer))
            sems = dict(send_sem=send_sems.at[N_REL * a + k - 1], recv_sem=recv_sems.at[N_REL * a + k - 1],
                        device_id=peer, device_id_type=MESH)
            sends.append(pltpu.make_async_remote_copy(src_ref=mine, dst_ref=into(land, me), **sems))
            waits.append(pltpu.make_async_remote_copy(src_ref=mine, dst_ref=into(land, _slot(peer)), **sems))
    return local, sends, waits


def _split_start(groups, name, after):
    built = []
    for view, srcs, lands in groups:
        lands = [lax.empty(l, s.dtype) if isinstance(l, tuple) else l for l, s in zip(lands, srcs)]
        built.append((view, list(srcs), lands))
    bufs = [b for _, srcs, lands in built for b in srcs + lands]
    nb, ng = len(bufs), len(built)

    def body(*refs):
        buf_refs, sem_refs, token = refs[:nb], refs[nb + 1:nb + 1 + 3 * ng], refs[-1]
        pos = 0
        for i, (view, srcs, _) in enumerate(built):
            n = len(srcs)
            local, sends, _ = _split_copies(view, buf_refs[pos:pos + n], buf_refs[pos + n:pos + 2 * n],
                                            *sem_refs[3 * i:3 * i + 3])
            pos += 2 * n
            for cp in local + sends:
                cp.start()
        token[...] = jnp.zeros_like(token)

    sems = []
    for _, srcs, _ in built:
        n = len(srcs)
        sems += [pltpu.SemaphoreType.DMA((N_REL * n,)), pltpu.SemaphoreType.DMA((N_REL * n,)),
                 pltpu.SemaphoreType.DMA((n,))]
    hbm = lambda a: pltpu.with_memory_space_constraint(a, pltpu.HBM)
    out = pl.pallas_call(
        body, name=name,
        out_shape=(*sems, *[pltpu.HBM(b.shape, b.dtype) for b in bufs], jax.ShapeDtypeStruct((8, DH), F32)),
        in_specs=[HBM] * nb + [ANY],
        out_specs=(*([SEM] * (3 * ng)), *([HBM] * nb), pl.BlockSpec(memory_space=pltpu.VMEM)),
        input_output_aliases={i: 3 * ng + i for i in range(nb)},
        compiler_params=pltpu.CompilerParams(has_side_effects=EFFECT),
    )(*[hbm(b) for b in bufs], after)
    handles, pos = [], 3 * ng
    for i, (view, srcs, _) in enumerate(built):
        n = len(srcs)
        handles.append(dict(view=view, n=n, sems=out[3 * i:3 * i + 3], srcs=list(out[pos:pos + n]),
                            lands=list(out[pos + n:pos + 2 * n])))
        pos += 2 * n
    return handles, out[-1]


def _split_wait(handle, name, after, srcs=None, lands=None):
    view, n, sems = handle["view"], handle["n"], handle["sems"]
    srcs = handle["srcs"] if srcs is None else srcs
    lands = handle["lands"] if lands is None else lands
    afters = list(after) if isinstance(after, (list, tuple)) else [after]

    def body(*refs):
        src_refs, land_refs = refs[:n], refs[n:2 * n]
        send_sems, recv_sems, local_sems = refs[2 * n:2 * n + 3]
        local, _, waits = _split_copies(view, src_refs, land_refs, send_sems, recv_sems, local_sems)
        for cp in waits:
            cp.wait_send()
            cp.wait_recv()
        for cp in local:
            cp.wait()

    out = pl.pallas_call(
        body, name=name,
        out_shape=(*[pltpu.HBM(s.shape, s.dtype) for s in srcs], *[pltpu.HBM(l.shape, l.dtype) for l in lands]),
        in_specs=[HBM] * (2 * n) + [SEM, SEM, SEM] + [ANY] * len(afters),
        out_specs=tuple([HBM] * (2 * n)),
        input_output_aliases={i: i for i in range(2 * n)},
        compiler_params=pltpu.CompilerParams(has_side_effects=EFFECT),
    )(*srcs, *lands, *sems, *afters)
    handle["srcs"] = list(out[:n])
    return list(out[n:])


def _tie(x, token, name):
    def body(x_ref, t_ref, o_ref):
        pass

    return pl.pallas_call(
        body, name=name, out_shape=jax.ShapeDtypeStruct(x.shape, x.dtype),
        in_specs=[ANY, ANY], out_specs=ANY, input_output_aliases={0: 0},
    )(x, token)


def _forward_to_sibling(land, name, rows):
    def body(land_ref, out_ref, send_sems, recv_sems):
        x, y, c = _place()
        sibling = (x, y, 1 - c)
        chips = [(1 - x, y), (x, 1 - y), (1 - x, 1 - y)]
        piece = pl.ds(rows[0], rows[1] - rows[0])

        def copy(j, core):
            blk = _slot((*chips[j], core))
            return pltpu.make_async_remote_copy(src_ref=land_ref.at[blk, piece], dst_ref=out_ref.at[blk, piece],
                                                send_sem=send_sems.at[j], recv_sem=recv_sems.at[j],
                                                device_id=sibling, device_id_type=MESH)

        sends = [copy(j, c) for j in range(3)]
        for cp in sends:
            cp.start()
        for j in range(3):
            copy(j, 1 - c).wait_recv()
        for cp in sends:
            cp.wait_send()

    return pl.pallas_call(
        body, name=name, in_specs=[ANY], out_specs=ANY, input_output_aliases={0: 0},
        out_shape=jax.ShapeDtypeStruct(land.shape, land.dtype),
        scratch_shapes=[pltpu.SemaphoreType.DMA((3,)), pltpu.SemaphoreType.DMA((3,))],
    )(land)


def _mod_fwd(a, w, b):
    def body(a_ref, w_ref, b_ref, o_ref):
        o_ref[...] = _dot(_silu(a_ref[...]), w_ref[...], NN, precision=HI) + b_ref[...]

    return pl.pallas_call(
        body, name="mod_fwd", out_shape=jax.ShapeDtypeStruct((a.shape[0], w.shape[1]), F32),
        compiler_params=pltpu.CompilerParams(vmem_limit_bytes=VMEM_LIMIT),
    )(a, w, b)


def _mod_bwd(a, d, w):
    def body(a_ref, d_ref, w_ref, dw_ref, dc_ref):
        av = a_ref[...]
        dv = d_ref[...]
        dw_ref[...] = _dot(_silu(av), dv, TN, precision=HI)
        da = _dot(dv[0:8, :], w_ref[...], NT, precision=HI) * _dsilu(av[0:8, :])
        row = lax.broadcasted_iota(jnp.int32, da.shape, 0)
        dc_ref[...] = jnp.where(row == 0, da, 0.0)

    return pl.pallas_call(
        body, name="mod_bwd",
        out_shape=[jax.ShapeDtypeStruct(w.shape, F32), jax.ShapeDtypeStruct((8, w.shape[0]), F32)],
        compiler_params=pltpu.CompilerParams(vmem_limit_bytes=VMEM_LIMIT),
    )(a, d, w)


def _sum_devices(g):
    def body(g_ref, o_ref):
        acc = g_ref[0]
        for i in range(1, g.shape[0]):
            acc = acc + g_ref[i]
        o_ref[...] = acc

    return pl.pallas_call(body, name="sum_devices_%d" % g.shape[1],
                          out_shape=jax.ShapeDtypeStruct(g.shape[1:], F32))(g)


def _sum_windows(g, name):
    n, r, c = g.shape
    tr = 128

    def body(g_ref, o_ref):
        x, y, cc = _place()
        lane0 = (W_IN_SHARD * _slot((x, y, cc))) % DH
        acc = g_ref[0].astype(F32)
        for i in range(1, n):
            acc = acc + g_ref[i].astype(F32)
        o_ref[...] = pltpu.roll(acc, (c - lane0) % c, 1).T

    return pl.pallas_call(
        body, name=name, grid=(r // tr,),
        in_specs=[pl.BlockSpec((n, tr, c), lambda i: (0, i, 0))],
        out_specs=pl.BlockSpec((c, tr), lambda i: (0, i)),
        out_shape=jax.ShapeDtypeStruct((c, r), F32),
        compiler_params=_cp("parallel"),
    )(g)


def _adam_rows(r, c, n):
    budget = 10 * 1024 * 1024
    best = None
    for tr in range(16, r + 1, 16):
        if r % tr == 0 and tr * c * (2 * n + 28) <= budget:
            best = tr
    return best if best is not None else r


def _adamw(g, w, m, v, name):
    n, r, c = g.shape
    tr = _adam_rows(r, c, n)
    bc1 = 1.0 - ADAM_B1 ** ADAM_STEP
    bc2 = 1.0 - ADAM_B2 ** ADAM_STEP

    def body(g_ref, w_ref, m_ref, v_ref, go_ref, d_ref, mo_ref, vo_ref):
        grad = g_ref[0].astype(F32)
        for i in range(1, n):
            grad = grad + g_ref[i].astype(F32)
        go_ref[...] = grad
        m_new = ADAM_B1 * m_ref[...] + (1.0 - ADAM_B1) * grad
        v_new = ADAM_B2 * v_ref[...] + (1.0 - ADAM_B2) * (grad * grad)
        mo_ref[...] = m_new
        vo_ref[...] = v_new
        d_ref[...] = -ADAM_LR * ((m_new / bc1) / (jnp.sqrt(v_new / bc2) + ADAM_EPS) + ADAM_WD * w_ref[...])

    blk = pl.BlockSpec((tr, c), lambda i: (i, 0))
    out = jax.ShapeDtypeStruct((r, c), F32)
    return pl.pallas_call(
        body, name=name, grid=(r // tr,),
        in_specs=[pl.BlockSpec((n, tr, c), lambda i: (0, i, 0)), blk, blk, blk],
        out_specs=[blk] * 4, out_shape=[out] * 4,
        compiler_params=_cp("parallel"),
    )(g, w, m, v)


ADAM_ROWS3 = 168


def _adam_math(grad, w, m, v):
    bc1 = 1.0 - ADAM_B1 ** ADAM_STEP
    bc2 = 1.0 - ADAM_B2 ** ADAM_STEP
    m_new = ADAM_B1 * m + (1.0 - ADAM_B1) * grad
    v_new = ADAM_B2 * v + (1.0 - ADAM_B2) * (grad * grad)
    delta = -ADAM_LR * ((m_new / bc1) / (jnp.sqrt(v_new / bc2) + ADAM_EPS) + ADAM_WD * w)
    return delta, m_new, v_new


def _adamw_rows3(g, w3, m3, v3, name, cols, prev):
    r, _, _ = w3.shape
    c = cols[1] - cols[0]
    n = min(-(-r // 16) * 8, ADAM_ROWS3 * D // c // 8 * 8)
    starts = list(range(0, r - n, n)) + [r - n]
    held = [] if prev is None else list(prev)

    def body(g_hbm, w_hbm, m_hbm, v_hbm, *refs):
        go_hbm, d_hbm, mo_hbm, vo_hbm, gbuf, ibuf, obuf, in_sems, out_sems = refs[len(held):]
        part = lambda h, r0: h.at[pl.ds(r0, n), 0, pl.ds(cols[0], c)]

        def fetch(p):
            r0, slot = starts[p], p % 2
            g0 = (r0 // 8) * 8
            cps = [pltpu.make_async_copy(g_hbm.at[pl.ds(g0, n + 8)], gbuf.at[slot], in_sems.at[slot, 0])]
            cps += [pltpu.make_async_copy(part(h, r0), ibuf.at[slot, k], in_sems.at[slot, 1 + k])
                    for k, h in enumerate((w_hbm, m_hbm, v_hbm))]
            for cp in cps:
                cp.start()
            return cps

        pending, outs = fetch(0), []
        for p, r0 in enumerate(starts):
            slot = p % 2
            nxt = fetch(p + 1) if p + 1 < len(starts) else []
            for cp in pending:
                cp.wait()
            grad = gbuf[slot, pl.ds(r0 - (r0 // 8) * 8, n), :]
            delta, m_new, v_new = _adam_math(grad, ibuf[slot, 0], ibuf[slot, 1], ibuf[slot, 2])
            for cp in outs:
                cp.wait()
            for k, val in enumerate((grad, delta, m_new, v_new)):
                obuf[slot, k] = val
            outs = [pltpu.make_async_copy(obuf.at[slot, k], part(h, r0), out_sems.at[slot, k])
                    for k, h in enumerate((go_hbm, d_hbm, mo_hbm, vo_hbm))]
            for cp in outs:
                cp.start()
            pending = nxt
        for cp in outs:
            cp.wait()

    out = jax.ShapeDtypeStruct(w3.shape, F32)
    return pl.pallas_call(
        body, name=name, in_specs=[ANY] * (4 + len(held)), out_specs=[ANY] * 4, out_shape=[out] * 4,
        input_output_aliases={4 + k: k for k in range(len(held))},
        scratch_shapes=[pltpu.VMEM((2, n + 8, c), F32), pltpu.VMEM((2, 3, n, c), F32), pltpu.VMEM((2, 4, n, c), F32),
                        pltpu.SemaphoreType.DMA((2, 4)), pltpu.SemaphoreType.DMA((2, 4))],
        compiler_params=pltpu.CompilerParams(vmem_limit_bytes=VMEM_LIMIT),
    )(g, w3, m3, v3, *held)


def kernel(x, c, ctx, c_ctx, w_mod, b_mod, norm_pre1, norm_post1, norm_pre2, norm_post2, w_in, hg_lb, hg_onorm, gla_w_gk, gla_b_gk, gla_onorm, w_br_hg, w_br_gla, w_out, w_ff_gate, w_ff_up, w_ff_down, loss_target, m_c_ctx, m_w_mod, m_b_mod, m_norm_pre1, m_norm_post1, m_norm_pre2, m_norm_post2, m_w_in, m_hg_lb, m_hg_onorm, m_gla_w_gk, m_gla_b_gk, m_gla_onorm, m_w_br_hg, m_w_br_gla, m_w_out, m_w_ff_gate, m_w_ff_up, m_w_ff_down, v_c_ctx, v_w_mod, v_b_mod, v_norm_pre1, v_norm_post1, v_norm_pre2, v_norm_post2, v_w_in, v_hg_lb, v_hg_onorm, v_gla_w_gk, v_gla_b_gk, v_gla_onorm, v_w_br_hg, v_w_br_gla, v_w_out, v_w_ff_gate, v_w_ff_up, v_w_ff_down):
    xi, yi, ci = lax.axis_index("x"), lax.axis_index("y"), lax.axis_index("c")
    me = 4 * xi + 2 * yi + ci
    t = CTX + x.shape[1]

    w_in_pieces, w_in_state = [], {}

    def w_in_piece(i):
        return (_view_near_rows((i * W_IN_PIECE, (i + 1) * W_IN_PIECE)), w_in_state["src"], w_in_state["land"])

    def started_w_in(handle):
        w_in_state.update(src=handle["srcs"], land=handle["lands"])
        w_in_pieces.append(handle)

    tr_ = lambda a: jnp.swapaxes(a[0], 0, 1)
    w_in_bf = jnp.pad(w_in[0].astype(BF16), ((0, 0), (0, W_IN_PAD - W_IN_SHARD)))
    w_in_state.update(src=[w_in_bf], land=[lax.empty((N_DEV,) + w_in_bf.shape, BF16)])
    gathered = lambda arrs: [(N_DEV,) + a.shape for a in arrs]
    whole = lambda arrs: (_view_whole, arrs, gathered(arrs))
    small_in = [c, hg_lb, gla_w_gk[0], gla_b_gk[0]]
    (small_handle, piece), tok = _split_start([whole(small_in), w_in_piece(0)], "ag_small_start", c)
    started_w_in(piece)
    c_all, lb_g, wgk_g, bgk_g = _split_wait(small_handle, "ag_small_wait", tok)
    big = [w_in[0], w_br_hg[0], w_br_gla[0], w_out[0], tr_(w_ff_gate), tr_(w_ff_up), w_ff_down[0]]
    big_bf = [None] + [w.astype(BF16) for w in big[1:]]
    cols = lambda g: jnp.transpose(g, (1, 0, 2)).reshape(g.shape[1], N_DEV * g.shape[2])

    def get_w_in(after):
        w_full, first = None, 0
        for s, last in enumerate(W_IN_STAGES):
            for i in range(first, last):
                land = _split_wait(w_in_pieces[i], "ag_w_in_wait%d" % i, after if w_full is None else [after, w_full],
                                   srcs=w_in_state["src"], lands=w_in_state["land"])
                w_in_state.update(src=w_in_pieces[i]["srcs"], land=land)
            rows = (first * W_IN_PIECE, last * W_IN_PIECE)
            w_in_state["land"] = [_forward_to_sibling(w_in_state["land"][0], "ag_w_in_forward%d" % s, rows)]
            w_full = _assemble_w_in(w_in_state["land"][0], rows, w_full, "assemble_w_in%d" % s)
            first = last
        return w_full

    def get_mix(after):
        g_brh, g_brg, g_out = _split_wait(mix_handle, "ag_mix_wait", after)
        return _gate_cols(cols(g_brh)), _gate_cols(cols(g_brg)), _gate_rows(g_out.reshape(D, D))

    def get_ffn(after):
        g_gate, g_up = _split_wait(ffn_handle, "ag_ffn_wait", after)

        def get_down(after):
            g_down, = _split_wait(down_handle, "ag_down_wait", after)
            return g_down.reshape(D_FF, D)

        return (g_gate.reshape(D_FF, D), g_up.reshape(D_FF, D)), get_down

    hg_lb_full = jnp.transpose(lb_g, (1, 2, 0, 3)).reshape(2, 2, HW)
    wgk_k = _layout_wgk(jnp.transpose(wgk_g, (1, 2, 0, 3)).reshape(2, 16, HW)).astype(BF16)
    bgk_k = jnp.transpose(bgk_g, (1, 0, 2)).reshape(1, D)
    onw = jnp.concatenate([jnp.tile(hg_onorm, (1, NH // 2)), jnp.tile(gla_onorm, (1, NH // 2))], axis=1)

    n_mod = w_mod.shape[2]
    a9 = jnp.concatenate([c_ctx[None], c_all[:, 0], jnp.zeros((16 - 1 - N_DEV, D), F32)], axis=0)
    b_loc = lax.dynamic_slice(b_mod, (0, me * n_mod), (1, n_mod))
    s_loc = _mod_fwd(a9, w_mod[0], b_loc)
    (mod_handle, piece), tok = _split_start([whole([s_loc]), w_in_piece(1)], "ag_mod_start", s_loc)
    started_w_in(piece)
    for i in range(2, D // W_IN_PIECE):
        (piece,), tok = _split_start([w_in_piece(i)], "ag_w_in_start%d" % i, tok)
        started_w_in(piece)
    s_all, = _split_wait(mod_handle, "ag_mod_wait", tok)
    mod_all = jnp.transpose(s_all, (1, 0, 2)).reshape(16, N_DEV * n_mod)
    pad8 = lambda m: jnp.concatenate([m.reshape(6, D), jnp.zeros((2, D), F32)], axis=0)
    modc = pad8(mod_all[0])
    modx = pad8(lax.dynamic_slice(mod_all, (1 + me, 0), (1, N_DEV * n_mod))[0])

    (mix_handle, ffn_handle, down_handle), tok = _split_start(
        [whole(big_bf[1:4]), whole(big_bf[4:6]), whole(big_bf[6:])], "ag_big_start", s_all)

    z = (ctx[0], x[0])
    modx = _tie(modx, tok, "tie_mod")
    norms = (norm_pre1, norm_post1, norm_pre2, norm_post2)
    rowshard = lambda d: d.reshape(N_DEV, d.shape[0] // N_DEV, d.shape[1]).astype(BF16)
    sent, w_in_grad = [], {}

    def w_in_chunk(i):
        half, rows = W_IN_GRAD_CHUNKS[i]
        return (_view_window(rows), w_in_grad[half], [(N_DEV, rows[1] - rows[0], D)])

    def sent_w_in(i, handle):
        w_in_grad[W_IN_GRAD_CHUNKS[i][0]] = handle["srcs"]
        sent.append(("w_in%d" % i, ["w_in#%d" % i], handle))

    def send(names, grads, x_after):
        if names == ("w_in_a",):
            w_in_grad["a"] = list(grads)
            (handle,), tok = _split_start([w_in_chunk(0)], "grads_w_in0_start", x_after)
            sent_w_in(0, handle)
            return _tie(x_after, tok, "tie_w_in0")
        if names == ("w_in_b",):
            w_in_grad["b"] = list(grads)
            return x_after
        arrs, leaves, col_arrs, col_leaves = [], [], [], []
        for nm, g in zip(names, grads):
            if nm in ("w_gate_t", "w_up_t"):
                arrs.append(rowshard(g))
                leaves.append({"w_gate_t": "w_ff_gate", "w_up_t": "w_ff_up"}[nm])
            elif nm == "w_down":
                arrs.append(rowshard(g))
                leaves.append("w_ff_down")
            elif nm == "w_out":
                arrs.append(rowshard(g[GOFF:GOFF + D]))
                leaves.append(nm)
            else:
                col_arrs.append(g[:, GOFF:GOFF + D])
                col_leaves.append(nm)
        groups = [(_view_block, arrs, [a.shape for a in arrs])]
        if col_arrs:
            groups.append((_view_cols, col_arrs, [(N_DEV, a.shape[0], D // N_DEV) for a in col_arrs]))
        handles, tok = _split_start(groups, "grads_%s_start" % names[0], x_after)
        sent.append((names[0], leaves, handles[0]))
        if col_arrs:
            sent.append((names[0] + "_cols", col_leaves, handles[1]))
        return _tie(x_after, tok, "tie_" + names[0])

    r = _local_step(z, loss_target[0], modc, modx, norms, onw, hg_lb_full, wgk_k, bgk_k,
                    get_w_in, get_mix, get_ffn, send)
    grad_x = r["grad_x"][None]

    sm_pre, sm_mid, sm_fin = r["sm_pre"], r["sm_mid"], r["sm_final"]
    dmodc = jnp.stack([sm_pre[0], sm_pre[2], sm_mid[4], sm_mid[0], sm_mid[2], sm_fin[0]]).reshape(-1)
    dmodx = jnp.stack([sm_pre[1], sm_pre[3], sm_mid[5], sm_mid[1], sm_mid[3], sm_fin[1]]).reshape(-1)
    on = r["sm_post"][0].reshape(NH, DH)
    pieces = [dmodc, dmodx, sm_pre[4], sm_mid[7], sm_mid[6], sm_fin[2], on[:NH // 2].sum(0), on[NH // 2:].sum(0),
              r["d_lb"][:2].reshape(-1), _unlayout_wgk(r["d_wgk"]).reshape(-1), r["d_bgk"][0]]
    loss_local = (0.5 / D) * jnp.sum(r["loss_vec"])
    pieces.append(jnp.concatenate([loss_local.reshape(1), jnp.zeros((DH - 1,), F32)]))
    sizes = [p.shape[0] for p in pieces]
    pack = jnp.concatenate(pieces).reshape(-1, DH)
    moms = [(m_w_in, v_w_in), (m_w_br_hg, v_w_br_hg), (m_w_br_gla, v_w_br_gla), (m_w_out, v_w_out),
            (m_w_ff_gate, v_w_ff_gate), (m_w_ff_up, v_w_ff_up), (m_w_ff_down, v_w_ff_down)]
    names = ["w_in", "w_br_hg", "w_br_gla", "w_out", "w_ff_gate", "w_ff_up", "w_ff_down"]
    wmv = {nm: (w, m, v) for nm, w, (m, v) in zip(names, big, moms)}
    res, updated = {}, {}

    def update(nm):
        w, m, v = wmv[nm]
        if nm in ("w_ff_gate", "w_ff_up"):
            outs = _adamw(recv[nm], w, tr_(m), tr_(v), "adamw_" + nm)
            res[nm] = [jnp.swapaxes(o, 0, 1)[None] for o in outs]
        else:
            outs = _adamw(recv[nm], w, m[0], v[0], "adamw_" + nm)
            res[nm] = [o[None] for o in outs]
        updated[nm] = outs[0]

    (small_handle, handle), tok = _split_start([whole([pack]), w_in_chunk(1)], "small_grads_start", pack)
    sent_w_in(1, handle)
    recv = {}
    for first, leaves, handle in sent:
        if not first.startswith("w_in"):
            recv.update(zip(leaves, _split_wait(handle, "grads_%s_wait" % first, tok)))
    update("w_ff_gate")
    update("w_ff_up")
    pack_all, = _split_wait(small_handle, "small_grads_wait", [updated["w_ff_gate"], updated["w_ff_up"]])
    tot = _sum_devices(pack_all).reshape(-1)
    offs = [sum(sizes[:i]) for i in range(len(sizes))]
    part = lambda i: tot[offs[i]:offs[i] + sizes[i]]
    dmodc_t, dmodx_t = part(0), part(1)
    g_b_mod = (dmodc_t + dmodx_t)[None]
    g_norms = [part(i)[None] for i in (2, 3, 4, 5)]
    g_hg_on, g_gla_on = part(6)[None], part(7)[None]
    lb0 = lax.dynamic_slice(part(8).reshape(2, HW), (0, me * (HW // N_DEV)), (2, HW // N_DEV))
    g_hg_lb = jnp.stack([lb0, -lb0])
    g_wgk = lax.dynamic_slice(part(9).reshape(2, 16, HW), (0, 0, me * (HW // N_DEV)), (2, 16, HW // N_DEV))[None]
    g_bgk = lax.dynamic_slice(part(10).reshape(2, HW), (0, me * (HW // N_DEV)), (2, HW // N_DEV))[None]
    loss = part(11)[0]

    dmx_all = pack_all.reshape(N_DEV, -1)[:, sizes[0]:sizes[0] + sizes[1]]
    d9 = jnp.concatenate([lax.dynamic_slice(dmodc_t[None], (0, me * n_mod), (1, n_mod)),
                          lax.dynamic_slice(dmx_all, (0, me * n_mod), (N_DEV, n_mod)),
                          jnp.zeros((16 - 1 - N_DEV, n_mod), F32)], axis=0)
    g_w_mod, dcc_part = _mod_bwd(a9, d9, w_mod[0])
    (cctx_handle, handle), tok = _split_start([whole([dcc_part]), w_in_chunk(2)], "c_ctx_start", dcc_part)
    sent_w_in(2, handle)
    recv["w_ff_down"] = _tie(recv["w_ff_down"], tok, "tie_down")
    update("w_ff_down")
    res["w_mod"] = [o[None] for o in _adamw(g_w_mod[None], w_mod[0], m_w_mod[0], v_w_mod[0], "adamw_w_mod")]
    for nm in ("w_out", "w_br_hg", "w_br_gla"):
        update(nm)
    dcc_all, = _split_wait(cctx_handle, "c_ctx_wait", [updated[nm] for nm in names[1:]] + [res["w_mod"][0]])
    g_c_ctx = _sum_devices(dcc_all)[0]

    small = [("c_ctx", c_ctx, m_c_ctx, v_c_ctx, g_c_ctx), ("b_mod", b_mod, m_b_mod, v_b_mod, g_b_mod),
             ("norm_pre1", norm_pre1, m_norm_pre1, v_norm_pre1, g_norms[0]),
             ("norm_post1", norm_post1, m_norm_post1, v_norm_post1, g_norms[1]),
             ("norm_pre2", norm_pre2, m_norm_pre2, v_norm_pre2, g_norms[2]),
             ("norm_post2", norm_post2, m_norm_post2, v_norm_post2, g_norms[3]),
             ("hg_lb", hg_lb, m_hg_lb, v_hg_lb, g_hg_lb), ("hg_onorm", hg_onorm, m_hg_onorm, v_hg_onorm, g_hg_on),
             ("gla_w_gk", gla_w_gk, m_gla_w_gk, v_gla_w_gk, g_wgk), ("gla_b_gk", gla_b_gk, m_gla_b_gk, v_gla_b_gk, g_bgk),
             ("gla_onorm", gla_onorm, m_gla_onorm, v_gla_onorm, g_gla_on)]
    flat = lambda k: jnp.concatenate([s[k].reshape(-1) for s in small]).reshape(-1, DH)
    outs = _adamw(flat(4)[None], flat(1), flat(2), flat(3), "adamw_small")
    off = 0
    for nm, w, _, _, _ in small:
        res[nm] = [o.reshape(-1)[off:off + w.size].reshape(w.shape) for o in outs]
        off += w.size

    done = [updated[nm] for nm in names[1:]] + [res["w_mod"][0]] + [o for nm, *_ in small for o in res[nm]]
    major = lambda a: jnp.transpose(a, (2, 0, 1))
    outs, row0 = None, 0
    for i, (first, leaves, handle) in enumerate(s for s in sent if s[0].startswith("w_in")):
        half = W_IN_GRAD_CHUNKS[i][0]
        land, = _split_wait(handle, "grads_%s_wait" % first, done, srcs=w_in_grad[half])
        w_in_grad[half] = handle["srcs"]
        rows = (row0, row0 + land.shape[1])
        outs = _adamw_rows3(_sum_windows(land, "sum_windows%d" % i), major(w_in), major(m_w_in), major(v_w_in),
                            "adamw_w_in%d" % i, rows, outs)
        row0 = rows[1]
    res["w_in"] = [jnp.transpose(o, (1, 2, 0)) for o in outs]

    order = ["c_ctx", "w_mod", "b_mod", "norm_pre1", "norm_post1", "norm_pre2", "norm_post2", "w_in", "hg_lb",
             "hg_onorm", "gla_w_gk", "gla_b_gk", "gla_onorm", "w_br_hg", "w_br_gla", "w_out", "w_ff_gate", "w_ff_up",
             "w_ff_down"]
    return (loss, grad_x, *[res[n][k] for k in range(4) for n in order])
```

```python
import functools

import jax
import jax.numpy as jnp
from jax import lax
from jax.experimental import pallas as pl
from jax.experimental.pallas import tpu as pltpu

F32 = jnp.float32
BF16 = jnp.bfloat16
HI = lax.Precision.HIGHEST

N_DEV = 8
D = 1024
CTX = 256
HW = 512
DH = 128
NH = 8
D_FF = 2816
EPS = 1e-6
GLA_NORM = 16.0
CHUNK = 64
TR = 256
NCT = CTX // TR
W_IN_COLS = 7168
MAIN0 = 0
LR0 = 4608
GW = 1152
GOFF = 32
GATE_HG0 = LR0
GATE_GLA0 = LR0 + D
LEVELS = (32, 16, 8)
EXP_CLAMP = 80.0
VMEM_LIMIT = 48 * 1024 * 1024

ADAM_LR, ADAM_B1, ADAM_B2, ADAM_EPS, ADAM_WD, ADAM_STEP = 0.001, 0.9, 0.999, 1e-08, 0.01, 10


def _cp(*sem):
    return pltpu.CompilerParams(dimension_semantics=sem, vmem_limit_bytes=VMEM_LIMIT)


def _sig(x):
    return jax.nn.sigmoid(x)


def _silu(x):
    return x * _sig(x)


def _dsilu(x):
    s = _sig(x)
    return s * (1.0 + x * (1.0 - s))


def _rstd(x):
    return lax.rsqrt(jnp.mean(x * x, axis=-1, keepdims=True) + EPS)


def _rms_bwd(a, y, r):
    return r * (a - y * (r * r) * jnp.mean(a * y, axis=-1, keepdims=True))


def _colsum(x):
    return jnp.sum(x, axis=0, keepdims=True)


def _dot(a, b, dims, precision=None):
    return lax.dot_general(a, b, (dims, ((), ())), preferred_element_type=F32, precision=precision)


NN = ((1,), (0,))
NT = ((1,), (1,))
TN = ((0,), (0,))

SCAN_HEADS_FWD = 4
SCAN_HEADS_BWD = 4


def _split_dot(m, x):
    mb = m.astype(BF16)
    x1 = x.astype(BF16)
    r1 = x - x1.astype(F32)
    x2 = r1.astype(BF16)
    x3 = (r1 - x2.astype(F32)).astype(BF16)
    return _dot(mb, x1, NN) + _dot(mb, x2, NN) + _dot(mb, x3, NN)


def _matmul(a, b, dims, out_dtype, name, tm, tn, tk, a_off=0, m_out=None):
    a_pair = isinstance(a, (tuple, list))
    as_ = list(a) if a_pair else [a]
    a = as_[0]
    pair = isinstance(b, (tuple, list))
    bs = list(b) if pair else [b]
    b1 = bs[0]
    rows = b1.shape[0] * len(bs)
    half = None
    if dims == NN:
        m, k, n = a.shape[0], rows, b1.shape[1]
        a_spec = pl.BlockSpec((tm, tk), lambda i, j, kk: (i, kk + a_off))
        half = b1.shape[0] // tk
        if a_pair:
            assert pair and a.shape[1] == b1.shape[0] and a_off == 0
            a_spec = [pl.BlockSpec((tm, tk), lambda i, j, kk: (i, jnp.minimum(kk, half - 1))),
                      pl.BlockSpec((tm, tk), lambda i, j, kk: (i, jnp.maximum(kk - half, 0)))]
        b_maps = [lambda i, j, kk: (kk, j)] if not pair else [
            lambda i, j, kk: (jnp.minimum(kk, half - 1), j), lambda i, j, kk: (jnp.maximum(kk - half, 0), j)]
        b_specs = [pl.BlockSpec((tk, tn), f) for f in b_maps]
        axis = 2
    elif dims == NT:
        m, k, n = a.shape[0], b1.shape[1], rows
        a_spec = pl.BlockSpec((tm, tk), lambda i, j, kk: (i, kk + a_off))
        half = b1.shape[0] // tn
        b_maps = [lambda i, j, kk: (j, kk)] if not pair else [
            lambda i, j, kk: (jnp.minimum(j, half - 1), kk), lambda i, j, kk: (jnp.maximum(j - half, 0), kk)]
        b_specs = [pl.BlockSpec((tn, tk), f) for f in b_maps]
        axis = 1
    else:
        assert not pair
        m, k = (a.shape[1] if m_out is None else m_out), a.shape[0]
        n = b1.shape[1]
        a_spec = pl.BlockSpec((tk, tm), lambda i, j, kk: (kk, i + a_off))
        b_specs = [pl.BlockSpec((tk, tn), lambda i, j, kk: (kk, j))]
    assert m % tm == 0 and n % tn == 0 and k % tk == 0, (name, m, n, k, tm, tn, tk)
    nk = k // tk
    nb = len(bs)
    na = len(as_)
    assert na == 1 or dims == NN

    def body(*refs):
        a_refs, refs = refs[:na], refs[na:]
        o_ref = refs[nb]
        if pair:
            bv = jnp.where(pl.program_id(axis) < half, refs[0][...], refs[1][...])
        else:
            bv = refs[0][...]
        av = a_refs[0][...] if na == 1 else jnp.where(pl.program_id(2) < half, a_refs[0][...], a_refs[1][...])
        part = _dot(av, bv, dims)
        if nk == 1:
            o_ref[...] = part.astype(o_ref.dtype)
            return
        acc_ref = refs[nb + 1]
        kk = pl.program_id(2)

        @pl.when(kk == 0)
        def _():
            acc_ref[...] = part

        @pl.when(kk > 0)
        def _():
            acc_ref[...] += part

        @pl.when(kk == nk - 1)
        def _():
            o_ref[...] = acc_ref[...].astype(o_ref.dtype)

    return pl.pallas_call(
        body,
        name=name,
        grid=(m // tm, n // tn, nk),
        in_specs=(a_spec if a_pair else [a_spec]) + b_specs,
        out_specs=pl.BlockSpec((tm, tn), lambda i, j, kk: (i, j)),
        out_shape=jax.ShapeDtypeStruct((m, n), out_dtype),
        scratch_shapes=[] if nk == 1 else [pltpu.VMEM((tm, tn), F32)],
        compiler_params=_cp("parallel", "parallel", "arbitrary"),
    )(*as_, *bs)


def _mm_gu_act(h, w_gate_t, w_up_t, name, tm):
    t = h.shape[0]
    tn = D_FF // 2

    def body(a_ref, bg_ref, bu_ref, u_ref, v_ref, act_ref):
        a = a_ref[...]
        u = _dot(a, bg_ref[...], NT)
        v = _dot(a, bu_ref[...], NT)
        u_ref[...] = u.astype(BF16)
        v_ref[...] = v.astype(BF16)
        act_ref[...] = (_silu(u) * v).astype(BF16)

    wspec = pl.BlockSpec((tn, D), lambda i, j: (j, 0))
    ospec = pl.BlockSpec((tm, tn), lambda i, j: (i, j))
    out = jax.ShapeDtypeStruct((t, D_FF), BF16)
    return pl.pallas_call(
        body, name=name, grid=(t // tm, D_FF // tn),
        in_specs=[pl.BlockSpec((tm, D), lambda i, j: (i, 0)), wspec, wspec],
        out_specs=[ospec] * 3, out_shape=[out] * 3,
        compiler_params=_cp("parallel", "parallel"),
    )(h, w_gate_t, w_up_t)


def _mm_down_dx_act(dy, w_down, u, v, name, tm):
    t = dy.shape[0]
    tn = D_FF // 2

    def body(a_ref, b_ref, u_ref, v_ref, du_ref, dv_ref):
        dact = _dot(a_ref[...], b_ref[...], NT)
        u = u_ref[...].astype(F32)
        du_ref[...] = (dact * v_ref[...].astype(F32) * _dsilu(u)).astype(BF16)
        dv_ref[...] = (dact * _silu(u)).astype(BF16)

    ospec = pl.BlockSpec((tm, tn), lambda i, j: (i, j))
    out = jax.ShapeDtypeStruct((t, D_FF), BF16)
    return pl.pallas_call(
        body, name=name, grid=(t // tm, D_FF // tn),
        in_specs=[pl.BlockSpec((tm, D), lambda i, j: (i, 0)), pl.BlockSpec((tn, D), lambda i, j: (j, 0)), ospec, ospec],
        out_specs=[ospec] * 2, out_shape=[out] * 2,
        compiler_params=_cp("parallel", "parallel"),
    )(dy, w_down, u, v)


def _row(c):
    return pl.BlockSpec((TR, c), lambda i: (i, 0))


def _rowcol(width, cb):
    return pl.BlockSpec((TR, width), lambda i: (i, cb))


def _full(shape):
    return pl.BlockSpec(shape, lambda i: (0,) * len(shape))


def _mod_row(mc_ref, mx_ref, k, is_ctx):
    return jnp.where(is_ctx, mc_ref[k:k + 1, :], mx_ref[k:k + 1, :])


def _z_specs():
    return [pl.BlockSpec((TR, D), lambda i: (jnp.minimum(i, NCT - 1), 0)),
            pl.BlockSpec((TR, D), lambda i: (jnp.maximum(i - NCT, 0), 0))]


def _z_tile(c_ref, x_ref, is_ctx):
    return jnp.where(is_ctx, c_ref[...], x_ref[...])


def _acc_row(ref, k, val):
    ref[k:k + 1, :] += val


def _acc_mod(ref, k, is_ctx, val):
    zero = jnp.zeros_like(val)
    ref[k:k + 1, :] += jnp.where(is_ctx, val, zero)
    ref[k + 1:k + 2, :] += jnp.where(is_ctx, zero, val)


def _prenorm(z, nw, modc, modx, i_shift, i_scale, name):
    t = z[0].shape[0] + z[1].shape[0]

    def body(zc_ref, zx_ref, nw_ref, mc_ref, mx_ref, h_ref):
        is_ctx = pl.program_id(0) < NCT
        x = _z_tile(zc_ref, zx_ref, is_ctx)
        n = x * _rstd(x) * nw_ref[...]
        h = n * (1.0 + _mod_row(mc_ref, mx_ref, i_scale, is_ctx)) + _mod_row(mc_ref, mx_ref, i_shift, is_ctx)
        h_ref[...] = h.astype(BF16)

    return pl.pallas_call(
        body, name=name, grid=(t // TR,),
        in_specs=_z_specs() + [_full((1, D)), _full((8, D)), _full((8, D))],
        out_specs=_row(D),
        out_shape=jax.ShapeDtypeStruct((t, D), BF16),
        compiler_params=_cp("parallel"),
    )(*z, nw, modc, modx)


def _hg_lb(lb_ref, d):
    a0 = lb_ref[0, d:d + 1, :]
    a1 = lb_ref[1, d:d + 1, :]
    mx = jnp.maximum(a0, a1)
    e0 = jnp.exp(a0 - mx)
    e1 = jnp.exp(a1 - mx)
    return e0 / (e0 + e1)


def _log_sigmoid(x):
    return jnp.minimum(x, 0.0) - jnp.log(1.0 + jnp.exp(-jnp.abs(x)))


def _gates_fwd(p, hg_lb, wgk, bgk):
    t = p.shape[0]
    seg = lambda j: _rowcol(HW, MAIN0 // HW + j)

    def body(hq_ref, hi_ref, hf_ref, hb_ref, gq_ref, gk_ref, gv_ref, lr_ref, lb_ref, wgk_ref, bgk_ref,
             q_ref, v_ref, kf_ref, kb_ref, gf_ref, gb_ref):
        q_ref[:, :HW] = _silu(hq_ref[...].astype(F32)).astype(BF16)
        q_ref[:, HW:] = (gq_ref[...].astype(F32) * (DH ** -0.5)).astype(BF16)
        v_ref[:, :HW] = hi_ref[...]
        v_ref[:, HW:] = gv_ref[...]
        xg = _dot(lr_ref[...].astype(BF16), wgk_ref[...], NN) + bgk_ref[...]
        for d, (raw_ref, k_ref, g_ref) in enumerate(((hf_ref, kf_ref, gf_ref), (hb_ref, kb_ref, gb_ref))):
            lbd = _hg_lb(lb_ref, d)
            f = lbd + (1.0 - lbd) * _sig(raw_ref[...].astype(F32))
            k_ref[:, :HW] = (1.0 - f).astype(BF16)
            k_ref[:, HW:] = gk_ref[...]
            g_ref[:, :HW] = jnp.log(f)
            g_ref[:, HW:] = _log_sigmoid(xg[:, d * HW:(d + 1) * HW]) * (1.0 / GLA_NORM)

    out = jax.ShapeDtypeStruct((t, D), F32)
    outb = jax.ShapeDtypeStruct((t, D), BF16)
    return pl.pallas_call(
        body, name="gates_fwd", grid=(t // TR,),
        in_specs=[seg(0), seg(1), seg(2), seg(3), seg(5), seg(6), seg(7), _rowcol(DH, LR0 // DH),
                  _full((2, 2, HW)), _full((DH, D)), _full((1, D))],
        out_specs=[_row(D)] * 6,
        out_shape=[outb] * 4 + [out] * 2,
        compiler_params=_cp("parallel"),
    )(p, p, p, p, p, p, p, p, hg_lb, wgk, bgk)


def _post_fwd(o_fw, o_bw, p, onw):
    t = o_fw.shape[0]

    def body(of_ref, ob_ref, g1_ref, g2_ref, w_ref, y_ref):
        for h in range(NH):
            sl = slice(h * DH, (h + 1) * DH)
            o = of_ref[:, sl] + ob_ref[:, sl]
            g_ref = g1_ref if h < NH // 2 else g2_ref
            gs = slice((h % (NH // 2)) * DH, (h % (NH // 2) + 1) * DH)
            n = o * _rstd(o) * w_ref[:, sl]
            y_ref[:, sl] = (n * _silu(g_ref[:, gs].astype(F32))).astype(BF16)

    return pl.pallas_call(
        body, name="post_fwd", grid=(t // TR,),
        in_specs=[_row(D), _row(D), _rowcol(HW, MAIN0 // HW + 4), _rowcol(HW, MAIN0 // HW + 8), _full((1, D))],
        out_specs=_row(D),
        out_shape=jax.ShapeDtypeStruct((t, D), BF16),
        compiler_params=_cp("parallel"),
    )(o_fw, o_bw, p, p, onw)


def _gate_window_specs(col0):
    return [_rowcol(HW, col0 // HW), _rowcol(HW, col0 // HW + 1), _rowcol(DH, (col0 + 2 * HW) // DH)]


def _gate_window(refs):
    return jnp.concatenate([r[...].astype(F32) for r in refs], axis=1)


def _branch_merge(y, w_hg, w_gla, p):
    t = y.shape[0]

    def body(y_ref, wh_ref, wg_ref, a0, a1, a2, b0, b1, b2, u1_ref, u2_ref, m_ref):
        u1 = _dot(y_ref[:, :HW], wh_ref[...], NN)
        u2 = _dot(y_ref[:, HW:], wg_ref[...], NN)
        u1_ref[...] = u1.astype(BF16)
        u2_ref[...] = u2.astype(BF16)
        m_ref[...] = (_sig(_gate_window((a0, a1, a2))) * u1 + _sig(_gate_window((b0, b1, b2))) * u2).astype(BF16)

    out = jax.ShapeDtypeStruct((t, GW), BF16)
    return pl.pallas_call(
        body, name="branch_merge", grid=(t // TR,),
        in_specs=[_row(D), _full((HW, GW)), _full((HW, GW))] + _gate_window_specs(GATE_HG0)
        + _gate_window_specs(GATE_GLA0),
        out_specs=[_row(GW)] * 3, out_shape=[out] * 3,
        compiler_params=_cp("parallel"),
    )(y, w_hg, w_gla, p, p, p, p, p, p)


def _mid_fwd(z, y1, nw_post, nw_pre, modc, modx):
    t = y1.shape[0]

    def body(zc_ref, zx_ref, y_ref, wpo_ref, wpr_ref, mc_ref, mx_ref, z1_ref, h_ref):
        is_ctx = pl.program_id(0) < NCT
        y = y_ref[...].astype(F32)
        z1 = _z_tile(zc_ref, zx_ref, is_ctx) + _mod_row(mc_ref, mx_ref, 2, is_ctx) * (y * _rstd(y) * wpo_ref[...])
        z1_ref[...] = z1
        n = z1 * _rstd(z1) * wpr_ref[...]
        h = n * (1.0 + _mod_row(mc_ref, mx_ref, 4, is_ctx)) + _mod_row(mc_ref, mx_ref, 3, is_ctx)
        h_ref[...] = h.astype(BF16)

    return pl.pallas_call(
        body, name="mid_fwd", grid=(t // TR,),
        in_specs=_z_specs() + [_row(D), _full((1, D)), _full((1, D)), _full((8, D)), _full((8, D))],
        out_specs=[_row(D), _row(D)],
        out_shape=[jax.ShapeDtypeStruct((t, D), F32), jax.ShapeDtypeStruct((t, D), BF16)],
        compiler_params=_cp("parallel"),
    )(*z, y1, nw_post, nw_pre, modc, modx)


def _final(z1, y2, target, nw, modc, modx):
    t = z1.shape[0]

    def body(z1_ref, y_ref, tg_ref, w_ref, mc_ref, mx_ref, dz_ref, dy_ref, loss_ref, sm_ref):
        i = pl.program_id(0)
        is_ctx = i < NCT

        @pl.when(i == 0)
        def _():
            loss_ref[...] = jnp.zeros_like(loss_ref)
            sm_ref[...] = jnp.zeros_like(sm_ref)

        g = _mod_row(mc_ref, mx_ref, 5, is_ctx)
        y = y_ref[...].astype(F32)
        r = _rstd(y)
        w = w_ref[...]
        yr = y * r
        n = yr * w
        e = z1_ref[...] + g * n - tg_ref[...]
        lat = jnp.where(is_ctx, 0.0, 1.0)
        loss_ref[...] += lat * _colsum(e * e)
        dz = e * (lat / D)
        dz_ref[...] = dz
        _acc_mod(sm_ref, 0, is_ctx, _colsum(dz * n))
        dn = dz * g
        _acc_row(sm_ref, 2, _colsum(dn * yr))
        dy_ref[...] = _rms_bwd(dn * w, y, r).astype(BF16)

    return pl.pallas_call(
        body, name="final", grid=(t // TR,),
        in_specs=[_row(D), _row(D), pl.BlockSpec((TR, D), lambda i: (jnp.maximum(i - NCT, 0), 0)),
                  _full((1, D)), _full((8, D)), _full((8, D))],
        out_specs=[_row(D), _row(D), _full((1, D)), _full((8, D))],
        out_shape=[jax.ShapeDtypeStruct((t, D), F32), jax.ShapeDtypeStruct((t, D), BF16),
                   jax.ShapeDtypeStruct((1, D), F32), jax.ShapeDtypeStruct((8, D), F32)],
        compiler_params=_cp("arbitrary"),
    )(z1, y2, target, nw, modc, modx)


def _mid_bwd(dh2, dz, z1, y1, nw_post, nw_pre, modc, modx):
    t = z1.shape[0]

    def body(dh_ref, dz_ref, z1_ref, y_ref, wpo_ref, wpr_ref, mc_ref, mx_ref, dzo_ref, dy_ref, sm_ref):
        i = pl.program_id(0)
        is_ctx = i < NCT

        @pl.when(i == 0)
        def _():
            sm_ref[...] = jnp.zeros_like(sm_ref)

        dh = dh_ref[...].astype(F32)
        z1 = z1_ref[...]
        r = _rstd(z1)
        zr = z1 * r
        wpr = wpr_ref[...]
        n = zr * wpr
        _acc_mod(sm_ref, 0, is_ctx, _colsum(dh))
        _acc_mod(sm_ref, 2, is_ctx, _colsum(dh * n))
        dn = dh * (1.0 + _mod_row(mc_ref, mx_ref, 4, is_ctx))
        _acc_row(sm_ref, 6, _colsum(dn * zr))
        dz1 = dz_ref[...] + _rms_bwd(dn * wpr, z1, r)
        dzo_ref[...] = dz1
        y = y_ref[...].astype(F32)
        r1 = _rstd(y)
        yr = y * r1
        wpo = wpo_ref[...]
        g = _mod_row(mc_ref, mx_ref, 2, is_ctx)
        _acc_mod(sm_ref, 4, is_ctx, _colsum(dz1 * (yr * wpo)))
        dn1 = dz1 * g
        _acc_row(sm_ref, 7, _colsum(dn1 * yr))
        dy_ref[...] = _rms_bwd(dn1 * wpo, y, r1).astype(BF16)

    return pl.pallas_call(
        body, name="mid_bwd", grid=(t // TR,),
        in_specs=[_row(D)] * 4 + [_full((1, D)), _full((1, D)), _full((8, D)), _full((8, D))],
        out_specs=[_row(D), _row(D), _full((8, D))],
        out_shape=[jax.ShapeDtypeStruct((t, D), F32), jax.ShapeDtypeStruct((t, D), BF16),
                   jax.ShapeDtypeStruct((8, D), F32)],
        compiler_params=_cp("arbitrary"),
    )(dh2, dz, z1, y1, nw_post, nw_pre, modc, modx)


def _pre_bwd(dh1, dz, z, nw, modc, modx):
    t = dh1.shape[0]

    def body(dh_ref, dz_ref, zc_ref, zx_ref, w_ref, mc_ref, mx_ref, dzo_ref, sm_ref):
        i = pl.program_id(0)
        is_ctx = i < NCT

        @pl.when(i == 0)
        def _():
            sm_ref[...] = jnp.zeros_like(sm_ref)

        dh = dh_ref[...].astype(F32)
        x = _z_tile(zc_ref, zx_ref, is_ctx)
        r = _rstd(x)
        xr = x * r
        w = w_ref[...]
        _acc_mod(sm_ref, 0, is_ctx, _colsum(dh))
        _acc_mod(sm_ref, 2, is_ctx, _colsum(dh * (xr * w)))
        dn = dh * (1.0 + _mod_row(mc_ref, mx_ref, 1, is_ctx))
        _acc_row(sm_ref, 4, _colsum(dn * xr))
        dzo_ref[...] = dz_ref[...] + _rms_bwd(dn * w, x, r)

    return pl.pallas_call(
        body, name="pre_bwd", grid=(t // TR,),
        in_specs=[_row(D)] * 2 + _z_specs() + [_full((1, D)), _full((8, D)), _full((8, D))],
        out_specs=[pl.BlockSpec((TR, D), lambda i: (jnp.maximum(i - NCT, 0), 0)), _full((8, D))],
        out_shape=[jax.ShapeDtypeStruct((t - CTX, D), F32), jax.ShapeDtypeStruct((8, D), F32)],
        compiler_params=_cp("arbitrary"),
    )(dh1, dz, *z, nw, modc, modx)


def _branch_merge_bwd(dm, p, u1, u2, w_hg, w_gla):
    t = dm.shape[0]

    def body(dm_ref, a0, a1, a2, b0, b1, b2, u1_ref, u2_ref, wh_ref, wg_ref, du1_ref, du2_ref, dg_ref, dyh_ref, dyg_ref):
        dm_ = dm_ref[...].astype(F32)
        s1 = _sig(_gate_window((a0, a1, a2)))
        s2 = _sig(_gate_window((b0, b1, b2)))
        du1 = (dm_ * s1).astype(BF16)
        du2 = (dm_ * s2).astype(BF16)
        du1_ref[...] = du1
        du2_ref[...] = du2
        dg_ref[:, :GW] = (dm_ * u1_ref[...].astype(F32) * s1 * (1.0 - s1)).astype(BF16)
        dg_ref[:, GW:] = (dm_ * u2_ref[...].astype(F32) * s2 * (1.0 - s2)).astype(BF16)
        dyh_ref[...] = _dot(du1, wh_ref[...], NT).astype(BF16)
        dyg_ref[...] = _dot(du2, wg_ref[...], NT).astype(BF16)

    return pl.pallas_call(
        body, name="branch_merge_bwd", grid=(t // TR,),
        in_specs=[_row(GW)] + _gate_window_specs(GATE_HG0) + _gate_window_specs(GATE_GLA0)
        + [_row(GW), _row(GW), _full((HW, GW)), _full((HW, GW))],
        out_specs=[_row(GW), _row(GW), _row(2 * GW), _row(HW), _row(HW)],
        out_shape=[jax.ShapeDtypeStruct((t, GW), BF16), jax.ShapeDtypeStruct((t, GW), BF16),
                   jax.ShapeDtypeStruct((t, 2 * GW), BF16), jax.ShapeDtypeStruct((t, HW), BF16),
                   jax.ShapeDtypeStruct((t, HW), BF16)],
        compiler_params=_cp("parallel"),
    )(dm, p, p, p, p, p, p, u1, u2, w_hg, w_gla)


def _post_bwd(dy_hg, dy_gla, o_fw, o_bw, p, onw):
    t = o_fw.shape[0]

    def body(d1_ref, d2_ref, of_ref, ob_ref, g1_ref, g2_ref, w_ref, do_ref, dg_ref, sm_ref):
        @pl.when(pl.program_id(0) == 0)
        def _():
            sm_ref[...] = jnp.zeros_like(sm_ref)

        for h in range(NH):
            sl = slice(h * DH, (h + 1) * DH)
            gs = slice((h % (NH // 2)) * DH, (h % (NH // 2) + 1) * DH)
            g_ref, d_ref = (g1_ref, d1_ref) if h < NH // 2 else (g2_ref, d2_ref)
            o = of_ref[:, sl] + ob_ref[:, sl]
            r = _rstd(o)
            orr = o * r
            w = w_ref[:, sl]
            gt = g_ref[:, gs].astype(F32)
            dy = d_ref[:, gs].astype(F32)
            dg_ref[:, sl] = (dy * (orr * w) * _dsilu(gt)).astype(BF16)
            dn = dy * _silu(gt)
            sm_ref[0:1, sl] += _colsum(dn * orr)
            do_ref[:, sl] = _rms_bwd(dn * w, o, r)

    return pl.pallas_call(
        body, name="post_bwd", grid=(t // TR,),
        in_specs=[_row(HW), _row(HW), _row(D), _row(D), _rowcol(HW, MAIN0 // HW + 4), _rowcol(HW, MAIN0 // HW + 8),
                  _full((1, D))],
        out_specs=[_row(D), _row(D), _full((8, D))],
        out_shape=[jax.ShapeDtypeStruct((t, D), F32), jax.ShapeDtypeStruct((t, D), BF16),
                   jax.ShapeDtypeStruct((8, D), F32)],
        compiler_params=_cp("arbitrary"),
    )(dy_hg, dy_gla, o_fw, o_bw, p, p, onw)


def _gates_bwd(p, hg_lb, wgk, bgk, dgm, dgo, dq_f, dq_b, dv_f, dv_b, dk_f, dk_b, dg_f, dg_b):
    t = p.shape[0]
    seg = lambda j: _rowcol(HW, MAIN0 // HW + j)

    def body(hq_ref, hf_ref, hb_ref, lr_ref, lb_ref, wgk_ref, bgk_ref, dgm_ref, dgo_ref,
             dqf_ref, dqb_ref, dvf_ref, dvb_ref, dkf_ref, dkb_ref, dgf_ref, dgb_ref,
             dp_ref, dlb_ref, dw_ref, db_ref):
        @pl.when(pl.program_id(0) == 0)
        def _():
            dlb_ref[...] = jnp.zeros_like(dlb_ref)
            dw_ref[...] = jnp.zeros_like(dw_ref)
            db_ref[...] = jnp.zeros_like(db_ref)

        c0 = MAIN0

        def put(j, val):
            dp_ref[:, c0 + j * HW:c0 + (j + 1) * HW] = val.astype(BF16)

        dq = dqf_ref[...].astype(F32) + dqb_ref[...].astype(F32)
        dv = dvf_ref[...].astype(F32) + dvb_ref[...].astype(F32)
        put(0, dq[:, :HW] * _dsilu(hq_ref[...].astype(F32)))
        put(1, dv[:, :HW])
        put(5, dq[:, HW:] * (DH ** -0.5))
        put(7, dv[:, HW:])
        put(6, dkf_ref[:, HW:].astype(F32) + dkb_ref[:, HW:].astype(F32))
        dp_ref[:, c0 + 4 * HW:c0 + 5 * HW] = dgo_ref[:, :HW]
        dp_ref[:, c0 + 8 * HW:c0 + 9 * HW] = dgo_ref[:, HW:]
        lr = lr_ref[...].astype(BF16)
        xg = _dot(lr, wgk_ref[...], NN) + bgk_ref[...]
        dxg = []
        for d, (raw_ref, dk_ref, dg_ref) in enumerate(((hf_ref, dkf_ref, dgf_ref), (hb_ref, dkb_ref, dgb_ref))):
            lbd = _hg_lb(lb_ref, d)
            s = _sig(raw_ref[...].astype(F32))
            f = lbd + (1.0 - lbd) * s
            df = dg_ref[:, :HW] / f - dk_ref[:, :HW].astype(F32)
            put(2 + d, df * (1.0 - lbd) * s * (1.0 - s))
            dlb_ref[d:d + 1, :] += _colsum(df * (1.0 - s)) * (lbd * (1.0 - lbd))
            dxg.append(dg_ref[:, HW:] * (1.0 / GLA_NORM) * _sig(-xg[:, d * HW:(d + 1) * HW]))
        dxg = jnp.concatenate(dxg, axis=1)
        db_ref[0:1, :] += _colsum(dxg)
        dxg_b = dxg.astype(BF16)
        dw_ref[...] += _dot(lr, dxg_b, TN)
        dlr = _dot(dxg_b, wgk_ref[...], NT)
        dp_ref[:, LR0:LR0 + DH] = (dlr + dgm_ref[:, :DH].astype(F32)).astype(BF16)
        dp_ref[:, LR0 + DH:GATE_GLA0] = dgm_ref[:, DH:D]
        dp_ref[:, GATE_GLA0:GATE_GLA0 + DH] = dgm_ref[:, D:GW] + dgm_ref[:, GW:GW + DH]
        dp_ref[:, GATE_GLA0 + DH:GATE_GLA0 + GW] = dgm_ref[:, GW + DH:]
        dp_ref[:, GATE_GLA0 + GW:] = jnp.zeros((TR, W_IN_COLS - GATE_GLA0 - GW), BF16)

    return pl.pallas_call(
        body, name="gates_bwd", grid=(t // TR,),
        in_specs=[seg(0), seg(2), seg(3), _rowcol(DH, LR0 // DH), _full((2, 2, HW)), _full((DH, D)), _full((1, D)),
                  _row(2 * GW), _row(D)] + [_row(D)] * 8,
        out_specs=[_row(W_IN_COLS), _full((8, HW)), _full((DH, D)), _full((8, D))],
        out_shape=[jax.ShapeDtypeStruct((t, W_IN_COLS), BF16), jax.ShapeDtypeStruct((8, HW), F32),
                   jax.ShapeDtypeStruct((DH, D), F32), jax.ShapeDtypeStruct((8, D), F32)],
        compiler_params=_cp("arbitrary"),
    )(p, p, p, p, hg_lb, wgk, bgk, dgm, dgo, dq_f, dq_b, dv_f, dv_b, dk_f, dk_b, dg_f, dg_b)


def _scan_consts(rev):
    r = lax.broadcasted_iota(jnp.int32, (CHUNK, CHUNK), 0)
    u = lax.broadcasted_iota(jnp.int32, (CHUNK, CHUNK), 1)
    rp = lax.broadcasted_iota(jnp.int32, (CHUNK, 1), 0)
    if rev:
        r, u, rp = CHUNK - 1 - r, CHUNK - 1 - u, CHUNK - 1 - rp
    tri = jnp.where(u <= r, 1.0, 0.0).astype(F32)
    tri_t = jnp.where(r <= u, 1.0, 0.0).astype(F32)
    lv = []
    for b in LEVELS:
        sh = b.bit_length() - 1
        pair = ((r >> sh) == (u >> sh) + 1) & (((u >> sh) & 1) == 0)
        pair_t = ((u >> sh) == (r >> sh) + 1) & (((r >> sh) & 1) == 0)
        tside = ((rp >> sh) & 1) == 1
        lv.append((pair, pair_t, tside, jnp.where(tside, 1.0, -1.0).astype(F32)))
    bd = LEVELS[-1].bit_length() - 1
    diag = ((r >> bd) == (u >> bd)) & (u <= r)
    diag_t = ((r >> bd) == (u >> bd)) & (r <= u)
    return tri, tri_t, lv, diag, diag_t


def _row_of(pos, rev):
    return CHUNK - 1 - pos if rev else pos


def _chunk_terms(cum, b_scr, consts, rev):
    _, _, lv, _, _ = consts
    terms = []
    for b, (_, _, _, sgn) in zip(LEVELS, lv):
        pieces = []
        for j in range(CHUNK // (2 * b)):
            row = _row_of(2 * b * j + b - 1, rev)
            pieces.append(jnp.broadcast_to(b_scr[row:row + 1, :], (2 * b, DH)))
        if rev:
            pieces = pieces[::-1]
        bnd = pieces[0] if len(pieces) == 1 else jnp.concatenate(pieces, axis=0)
        terms.append(jnp.exp((cum - bnd) * sgn))
    b = LEVELS[-1]
    pieces = []
    for j in range(CHUNK // b):
        if j == 0:
            pieces.append(jnp.zeros((b, DH), F32))
        else:
            row = _row_of(b * j - 1, rev)
            pieces.append(jnp.broadcast_to(b_scr[row:row + 1, :], (b, DH)))
    if rev:
        pieces = pieces[::-1]
    start = jnp.concatenate(pieces, axis=0)
    wq = jnp.exp(jnp.minimum(cum - start, 0.0))
    wk = jnp.exp(jnp.minimum(start - cum, EXP_CLAMP))
    terms.append((wq, wk))
    return terms


def _run_staged(units):
    live = list(units)
    while live:
        nxt = []
        for u in live:
            try:
                next(u)
                nxt.append(u)
            except StopIteration:
                pass
        live = nxt


SCAN_TB = 256
SCAN_CB = SCAN_TB // CHUNK


def _block_order(i, ntb, rev):
    nctx = CTX // SCAN_TB
    if not rev:
        return i
    return jnp.where(i < nctx, nctx - 1 - i, ntb - 1 - (i - nctx))


def _chunk_in_block(j, rev):
    return SCAN_CB - 1 - j if rev else j


def _scan_fwd(q, k, v, g, rev):
    t = q.shape[0]
    nc = t // CHUNK
    hpb = SCAN_HEADS_FWD

    def body(q_ref, k_ref, v_ref, g_ref, o_ref, st_ref, s_scr, b_scr):
        consts = _scan_consts(rev)
        _, _, lv, diag, _ = consts
        masks = [lvl[0] for lvl in lv] + [diag]

        @pl.when(pl.program_id(1) == 0)
        def _():
            s_scr[...] = jnp.zeros_like(s_scr)

        tri = consts[0]
        state = {hh: s_scr[hh] for hh in range(hpb)}

        def unit(hh, j):
            sl = slice(hh * DH, (hh + 1) * DH)
            c = _chunk_in_block(j, rev)
            rows = slice(c * CHUNK, (c + 1) * CHUNK)
            b_ref = b_scr.at[hh * SCAN_CB + j]
            qc, kc, vc, gc = q_ref[rows, sl], k_ref[rows, sl], v_ref[rows, sl], g_ref[rows, sl]
            cum = _split_dot(tri, gc)
            b_ref[...] = cum
            yield
            terms = _chunk_terms(cum, b_ref, consts, rev)
            qf, kf = qc.astype(F32), kc.astype(F32)
            xs = [(jnp.where(tside, qf, kf) * w).astype(BF16) for w, (_, _, tside, _) in zip(terms[:-1], lv)]
            qd, kd = (qf * terms[-1][0]).astype(BF16), (kf * terms[-1][1]).astype(BF16)
            tot = _colsum(gc)
            qe = (qf * jnp.exp(cum)).astype(BF16)
            ke = (kf * jnp.exp(tot - cum)).astype(BF16)
            vb = vc.astype(BF16)
            yield
            scs = [_dot(x, x, NT) for x in xs] + [_dot(qd, kd, NT)]
            kv = _dot(vb, ke, TN)
            yield
            a = jnp.zeros((CHUNK, CHUNK), F32)
            for sc, m in zip(scs, masks):
                a = a + jnp.where(m, sc, 0.0)
            o_intra = _dot(a.astype(BF16), vb, NN)
            yield
            st = state[hh]
            st_ref[hh, c] = st
            o_ref[rows, sl] = o_intra + _dot(qe, st.astype(BF16), NT)
            state[hh] = st * jnp.exp(tot) + kv
            yield

        _run_staged([unit(hh, j) for hh in range(hpb) for j in range(SCAN_CB)])
        for hh in range(hpb):
            s_scr[hh] = state[hh]

    ntb = t // SCAN_TB
    col = pl.BlockSpec((SCAN_TB, hpb * DH), lambda h, i: (_block_order(i, ntb, rev), h))
    return pl.pallas_call(
        body, name="scan_fwd_" + ("bw" if rev else "fw"), grid=(NH // hpb, ntb),
        in_specs=[col] * 4,
        out_specs=[col, pl.BlockSpec((hpb, SCAN_CB, DH, DH), lambda h, i: (h, _block_order(i, ntb, rev), 0, 0))],
        out_shape=[jax.ShapeDtypeStruct((t, D), F32), jax.ShapeDtypeStruct((NH, nc, DH, DH), F32)],
        scratch_shapes=[pltpu.VMEM((hpb, DH, DH), F32), pltpu.VMEM((hpb * SCAN_CB, CHUNK, DH), F32)],
        compiler_params=_cp("parallel", "arbitrary"),
    )(q, k, v, g)


def _scan_bwd(q, k, v, g, do, states, rev):
    t = q.shape[0]
    nc = t // CHUNK
    hpb = SCAN_HEADS_BWD

    def body(q_ref, k_ref, v_ref, g_ref, do_ref, st_ref, dq_ref, dk_ref, dv_ref, dg_ref, ds_scr, b_scr):
        consts = _scan_consts(rev)
        _, tri_t, lv, diag, diag_t = consts
        masks = [(lvl[0], lvl[1]) for lvl in lv] + [(diag, diag_t)]
        @pl.when(pl.program_id(1) == 0)
        def _():
            ds_scr[...] = jnp.zeros_like(ds_scr)

        tri = consts[0]
        dstate = {hh: ds_scr[hh] for hh in range(hpb)}

        def unit(hh, jj):
            sl = slice(hh * DH, (hh + 1) * DH)
            c = _chunk_in_block(SCAN_CB - 1 - jj, rev)
            rows = slice(c * CHUNK, (c + 1) * CHUNK)
            b_ref = b_scr.at[hh * SCAN_CB + jj]
            qc, kc, vc, gc = q_ref[rows, sl], k_ref[rows, sl], v_ref[rows, sl], g_ref[rows, sl]
            dob = do_ref[rows, sl].astype(BF16)
            vb = vc.astype(BF16)
            cum = _split_dot(tri, gc)
            b_ref[...] = cum
            da = _dot(dob, vb, NT)
            da_t = _dot(vb, dob, NT)
            yield
            terms = _chunk_terms(cum, b_ref, consts, rev)
            qf, kf = qc.astype(F32), kc.astype(F32)
            xs = [(jnp.where(tside, qf, kf) * w).astype(BF16) for w, (_, _, tside, _) in zip(terms[:-1], lv)]
            wqd, wkd = terms[-1]
            qdb, kdb = (qf * wqd).astype(BF16), (kf * wkd).astype(BF16)
            tot = _colsum(gc)
            e_tot = jnp.exp(tot)
            e_b = jnp.exp(cum)
            e_t = jnp.exp(tot - cum)
            qeb = (qf * e_b).astype(BF16)
            keb = (kf * e_t).astype(BF16)
            dsym = [(jnp.where(m, da, 0.0) + jnp.where(m_t, da_t, 0.0)).astype(BF16) for m, m_t in masks[:-1]]
            dad = (jnp.where(diag, da, 0.0).astype(BF16), jnp.where(diag_t, da_t, 0.0).astype(BF16))
            yield
            sym = [_dot(x, x, NT) for x in xs]
            dxs = [_dot(d, x, NN) for d, x in zip(dsym, xs)]
            at_d = _dot(kdb, qdb, NT)
            dqt_d = _dot(dad[0], kdb, NN)
            dkt_d = _dot(dad[1], qdb, NN)
            qd = _dot(dob, qeb, TN)
            yield
            a_t = jnp.where(diag_t, at_d, 0.0)
            dq = dqt_d * wqd
            dk = dkt_d * wkd
            db = dqt_d * qdb.astype(F32) - dkt_d * kdb.astype(F32)
            for s, dx, x, w, (_, m_t, tside, sgn) in zip(sym, dxs, xs, terms[:-1], lv):
                a_t = a_t + jnp.where(m_t, s, 0.0)
                dxw = dx * w
                dq = dq + jnp.where(tside, dxw, 0.0)
                dk = dk + jnp.where(tside, 0.0, dxw)
                db = db + (dx * x.astype(F32)) * sgn
            dv_intra = _dot(a_t.astype(BF16), dob, NN)
            st = st_ref[hh, c]
            stb = st.astype(BF16)
            dqe = _dot(dob, stb, NN)
            yield
            dst = dstate[hh]
            dstb = dst.astype(BF16)
            dstate[hh] = dst * e_tot + qd
            dv_ref[rows, sl] = (dv_intra + _dot(keb, dstb, NT)).astype(BF16)
            dke = _dot(vb, dstb, NN)
            yield
            qe = qeb.astype(F32)
            ke = keb.astype(F32)
            dq_ref[rows, sl] = (dq + dqe * e_b).astype(BF16)
            dk_ref[rows, sl] = (dk + dke * e_t).astype(BF16)
            db = db + dqe * qe - dke * ke
            dtot = _colsum(dstb.astype(F32) * stb.astype(F32)) * e_tot + _colsum(dke * ke)
            dg_ref[rows, sl] = _split_dot(tri_t, db) + dtot
            yield

        _run_staged([unit(hh, jj) for hh in range(hpb) for jj in range(SCAN_CB)])
        for hh in range(hpb):
            ds_scr[hh] = dstate[hh]

    ntb = t // SCAN_TB
    blk = lambda i: _block_order(ntb - 1 - i, ntb, rev)
    col = pl.BlockSpec((SCAN_TB, hpb * DH), lambda h, i: (blk(i), h))
    out = jax.ShapeDtypeStruct((t, D), F32)
    outb = jax.ShapeDtypeStruct((t, D), BF16)
    return pl.pallas_call(
        body, name="scan_bwd_" + ("bw" if rev else "fw"), grid=(NH // hpb, ntb),
        in_specs=[col] * 5 + [pl.BlockSpec((hpb, SCAN_CB, DH, DH), lambda h, i: (h, blk(i), 0, 0))],
        out_specs=[col] * 4,
        out_shape=[outb] * 3 + [out],
        scratch_shapes=[pltpu.VMEM((hpb, DH, DH), F32), pltpu.VMEM((hpb * SCAN_CB, CHUNK, DH), F32)],
        compiler_params=_cp("parallel", "arbitrary"),
    )(q, k, v, g, do, states)


W_IN_GRAD_CHUNKS = (("a", (0, 512)), ("b", (0, 256)), ("b", (256, 512)))
W_IN_REF = 6688
W_IN_PAD = 896
W_IN_PIECE = 256
W_IN_STAGES = (3, 4)


def _assemble_w_in(g, rows, prev, name):
    n, r, wp = g.shape
    tr = W_IN_PIECE
    tiles = wp // DH
    first = rows[0] // tr

    def body(g_ref, *refs):
        o_ref = refs[-1]
        lane = lax.broadcasted_iota(jnp.int32, (tr, DH), 1)
        for t in range(W_IN_COLS // DH):
            acc = None
            for j in range(n):
                c = DH * t - W_IN_SHARD * j
                if c <= -DH or c >= W_IN_SHARD:
                    continue
                k, s = divmod(c, DH)
                lo = g_ref[j, :, k * DH:(k + 1) * DH] if 0 <= k < tiles else None
                hi = g_ref[j, :, (k + 1) * DH:(k + 2) * DH] if s and 0 <= k + 1 < tiles else None
                if s:
                    zero = jnp.zeros((tr, DH), g.dtype)
                    lo = zero if lo is None else pltpu.roll(lo, DH - s, 1)
                    hi = zero if hi is None else pltpu.roll(hi, DH - s, 1)
                    part = jnp.where(lane < DH - s, lo, hi)
                else:
                    part = lo
                acc = part if acc is None else acc + part
            o_ref[:, t * DH:(t + 1) * DH] = jnp.zeros((tr, DH), g.dtype) if acc is None else acc

    held = [] if prev is None else [prev]
    return pl.pallas_call(
        body, name=name, grid=((rows[1] - rows[0]) // tr,),
        in_specs=[pl.BlockSpec((n, tr, wp), lambda i: (0, first + i, 0))] + [pl.BlockSpec(memory_space=pl.ANY)] * len(held),
        out_specs=pl.BlockSpec((tr, W_IN_COLS), lambda i: (first + i, 0)),
        out_shape=jax.ShapeDtypeStruct((r, W_IN_COLS), g.dtype),
        input_output_aliases={1: 0} if held else {},
        compiler_params=_cp("parallel"),
    )(g, *held)


def _gate_cols(w):
    return jnp.pad(w, ((0, 0), (GOFF, GW - GOFF - D)))


def _gate_rows(w):
    return jnp.pad(w, ((GOFF, GW - GOFF - D), (0, 0)))


def _layout_wgk(w):
    r = w.shape[1]
    top = jnp.concatenate([w[0], jnp.zeros_like(w[0])], axis=1)
    bot = jnp.concatenate([jnp.zeros_like(w[1]), w[1]], axis=1)
    return jnp.concatenate([top, bot, jnp.zeros((DH - 2 * r, D), w.dtype)], axis=0)


def _unlayout_wgk(d, r=16):
    return jnp.stack([d[:r, :HW], d[r:2 * r, HW:]])


def _local_step(z, target, modc, modx, norms, onw, hg_lb, wgk, bgk, get_w_in, get_mix, get_ffn, send):
    n_pre1, n_post1, n_pre2, n_post2 = norms
    t = z[0].shape[0] + z[1].shape[0]
    tm = 1152 if t % 1152 == 0 else 256
    h1 = _prenorm(z, n_pre1, modc, modx, 0, 1, "prenorm1")
    w_in = get_w_in(h1)
    p = _matmul(h1, w_in, NN, BF16, "mm_in", t, 1024, D)
    q, v, k_f, k_b, g_f, g_b = _gates_fwd(p, hg_lb, wgk, bgk)
    o_f, st_f = _scan_fwd(q, k_f, v, g_f, False)
    o_b, st_b = _scan_fwd(q, k_b, v, g_b, True)
    y = _post_fwd(o_f, o_b, p, onw)
    w_br_hg, w_br_gla, w_out = get_mix(y)
    u1, u2, merged = _branch_merge(y, w_br_hg, w_br_gla, p)
    y1 = _matmul(merged, w_out, NN, BF16, "mm_out", tm, 512, GW)
    z1, h2 = _mid_fwd(z, y1, n_post1, n_pre2, modc, modx)
    w_gu_t, get_down = get_ffn(h2)
    u, v_ff, act = _mm_gu_act(h2, w_gu_t[0], w_gu_t[1], "mm_gu", tm)
    w_down = get_down(act)
    y2 =_matmul(act, w_down, NN, BF16, "mm_down", t, 512, D_FF)
    dz, dy2, loss_vec, sm_final = _final(z1, y2, target, n_post2, modc, modx)
    du, dv_ff = _mm_down_dx_act(dy2, w_down, u, v_ff, "mm_down_dx", tm)
    d_w_down = _matmul(act, dy2, TN, BF16, "mm_down_dw", D_FF // 2, 1024, t)
    dh2 = _matmul((du, dv_ff), w_gu_t, NN, BF16, "mm_gu_dx", tm, 512, D_FF)
    d_w_gate_t = _matmul(du, h2, TN, BF16, "mm_gate_dw", D_FF // 2, 1024, t)
    d_w_up_t = _matmul(dv_ff, h2, TN, BF16, "mm_up_dw", D_FF // 2, 1024, t)
    dh2 = send(("w_down", "w_gate_t", "w_up_t"), (d_w_down, d_w_gate_t, d_w_up_t), dh2)
    dz, dy1, sm_mid = _mid_bwd(dh2, dz, z1, y1, n_post1, n_pre2, modc, modx)
    dmerged = _matmul(dy1, w_out, NT, BF16, "mm_out_dx", tm, GW, D)
    d_w_out = _matmul(merged, dy1, TN, BF16, "mm_out_dw", GW, 512, t)
    du1, du2, dgm, dy_hg, dy_gla = _branch_merge_bwd(dmerged, p, u1, u2, w_br_hg, w_br_gla)
    d_w_br_hg = _matmul(y, du1, TN, BF16, "mm_br_hg_dw", HW, GW, t, a_off=0, m_out=HW)
    d_w_br_gla = _matmul(y, du2, TN, BF16, "mm_br_gla_dw", HW, GW, t, a_off=1, m_out=HW)
    dy_hg = send(("w_out", "w_br_hg", "w_br_gla"), (d_w_out, d_w_br_hg, d_w_br_gla), dy_hg)
    do, dgo, sm_post = _post_bwd(dy_hg, dy_gla, o_f, o_b, p, onw)
    dq_f, dk_f, dv_f, dg_f = _scan_bwd(q, k_f, v, g_f, do, st_f, False)
    dq_b, dk_b, dv_b, dg_b = _scan_bwd(q, k_b, v, g_b, do, st_b, True)
    dp, d_lb, d_wgk, d_bgk = _gates_bwd(p, hg_lb, wgk, bgk, dgm, dgo, dq_f, dq_b, dv_f, dv_b, dk_f, dk_b, dg_f, dg_b)
    d_w_in_a = _matmul(h1, dp, TN, BF16, "mm_in_dw_a", 512, 1024, t, a_off=0, m_out=D // 2)
    dp = send(("w_in_a",), (d_w_in_a,), dp)
    d_w_in_b = _matmul(h1, dp, TN, BF16, "mm_in_dw_b", 512, 1024, t, a_off=1, m_out=D // 2)
    dp = send(("w_in_b",), (d_w_in_b,), dp)
    dh1 = _matmul(dp, w_in, NT, BF16, "mm_in_dx", tm, 512, W_IN_COLS // 2)
    grad_x, sm_pre = _pre_bwd(dh1, dz, z, n_pre1, modc, modx)
    return dict(loss_vec=loss_vec, grad_x=grad_x, sm_final=sm_final, sm_mid=sm_mid, sm_post=sm_post, sm_pre=sm_pre,
                d_lb=d_lb, d_wgk=d_wgk, d_bgk=d_bgk)


MESH = pl.DeviceIdType.MESH
ANY = pl.BlockSpec(memory_space=pl.ANY)
N_REL = N_DEV - 1


def _place():
    return lax.axis_index("x"), lax.axis_index("y"), lax.axis_index("c")


def _slot(p):
    return 4 * p[0] + 2 * p[1] + p[2]


HBM = pl.BlockSpec(memory_space=pltpu.HBM)
SEM = pl.BlockSpec(memory_space=pltpu.SEMAPHORE)
EFFECT = pltpu.SideEffectType.DATAFLOW_SIDE_EFFECTING


def _peer_of(x, y, c, k):
    flip = lambda v, bit: 1 - v if bit else v
    return flip(x, k & 4), flip(y, k & 2), flip(c, k & 1)


def _view_whole(src, slot):
    return src


def _view_near(src, slot):
    return src


_view_near.peers = (1, 2, 4)


def _view_near_rows(rows):
    def view(src, slot):
        return src.at[pl.ds(rows[0], rows[1] - rows[0])]
    view.peers = _view_near.peers
    view.land = lambda land, slot: land.at[slot, pl.ds(rows[0], rows[1] - rows[0])]
    return view


def _view_block(src, slot):
    return src.at[slot]


def _view_cols(src, slot):
    return src.at[:, pl.ds(pl.multiple_of(slot * (D // N_DEV), D // N_DEV), D // N_DEV)]


W_IN_SHARD = W_IN_REF // N_DEV


def _view_window(rows):
    def view(src, slot):
        col0 = pl.multiple_of((W_IN_SHARD * slot // DH) * DH, DH)
        return src.at[pl.ds(rows[0], rows[1] - rows[0]), pl.ds(col0, D)]
    return view


def _split_copies(view, srcs, lands, send_sems, recv_sems, local_sems):
    x, y, c = _place()
    me = _slot((x, y, c))
    into = getattr(view, "land", lambda land, slot: land.at[slot])
    local, sends, waits = [], [], []
    for a, (src, land) in enumerate(zip(srcs, lands)):
        local.append(pltpu.make_async_copy(view(src, me), into(land, me), local_sems.at[a]))
        for k in getattr(view, "peers", range(1, N_DEV)):
            peer = _peer_of(x, y, c, k)
            mine = view(src, _slot(peer))
            sems = dict(send_sem=send_sems.at[N_REL * a + k - 1], recv_sem=recv_sems.at[N_REL * a + k - 1],
                        device_id=peer, device_id_type=MESH)
            sends.append(pltpu.make_async_remote_copy(src_ref=mine, dst_ref=into(land, me), **sems))
            waits.append(pltpu.make_async_remote_copy(src_ref=mine, dst_ref=into(land, _slot(peer)), **sems))
    return local, sends, waits


def _split_start(groups, name, after):
    built = []
    for view, srcs, lands in groups:
        lands = [lax.empty(l, s.dtype) if isinstance(l, tuple) else l for l, s in zip(lands, srcs)]
        built.append((view, list(srcs), lands))
    bufs = [b for _, srcs, lands in built for b in srcs + lands]
    nb, ng = len(bufs), len(built)

    def body(*refs):
        buf_refs, sem_refs, token = refs[:nb], refs[nb + 1:nb + 1 + 3 * ng], refs[-1]
        pos = 0
        for i, (view, srcs, _) in enumerate(built):
            n = len(srcs)
            local, sends, _ = _split_copies(view, buf_refs[pos:pos + n], buf_refs[pos + n:pos + 2 * n],
                                            *sem_refs[3 * i:3 * i + 3])
            pos += 2 * n
            for cp in local + sends:
                cp.start()
        token[...] = jnp.zeros_like(token)

    sems = []
    for _, srcs, _ in built:
        n = len(srcs)
        sems += [pltpu.SemaphoreType.DMA((N_REL * n,)), pltpu.SemaphoreType.DMA((N_REL * n,)),
                 pltpu.SemaphoreType.DMA((n,))]
    hbm = lambda a: pltpu.with_memory_space_constraint(a, pltpu.HBM)
    out = pl.pallas_call(
        body, name=name,
        out_shape=(*sems, *[pltpu.HBM(b.shape, b.dtype) for b in bufs], jax.ShapeDtypeStruct((8, DH), F32)),
        in_specs=[HBM] * nb + [ANY],
        out_specs=(*([SEM] * (3 * ng)), *([HBM] * nb), pl.BlockSpec(memory_space=pltpu.VMEM)),
        input_output_aliases={i: 3 * ng + i for i in range(nb)},
        compiler_params=pltpu.CompilerParams(has_side_effects=EFFECT),
    )(*[hbm(b) for b in bufs], after)
    handles, pos = [], 3 * ng
    for i, (view, srcs, _) in enumerate(built):
        n = len(srcs)
        handles.append(dict(view=view, n=n, sems=out[3 * i:3 * i + 3], srcs=list(out[pos:pos + n]),
                            lands=list(out[pos + n:pos + 2 * n])))
        pos += 2 * n
    return handles, out[-1]


def _split_wait(handle, name, after, srcs=None, lands=None):
    view, n, sems = handle["view"], handle["n"], handle["sems"]
    srcs = handle["srcs"] if srcs is None else srcs
    lands = handle["lands"] if lands is None else lands
    afters = list(after) if isinstance(after, (list, tuple)) else [after]

    def body(*refs):
        src_refs, land_refs = refs[:n], refs[n:2 * n]
        send_sems, recv_sems, local_sems = refs[2 * n:2 * n + 3]
        local, _, waits = _split_copies(view, src_refs, land_refs, send_sems, recv_sems, local_sems)
        for cp in waits:
            cp.wait_send()
            cp.wait_recv()
        for cp in local:
            cp.wait()

    out = pl.pallas_call(
        body, name=name,
        out_shape=(*[pltpu.HBM(s.shape, s.dtype) for s in srcs], *[pltpu.HBM(l.shape, l.dtype) for l in lands]),
        in_specs=[HBM] * (2 * n) + [SEM, SEM, SEM] + [ANY] * len(afters),
        out_specs=tuple([HBM] * (2 * n)),
        input_output_aliases={i: i for i in range(2 * n)},
        compiler_params=pltpu.CompilerParams(has_side_effects=EFFECT),
    )(*srcs, *lands, *sems, *afters)
    handle["srcs"] = list(out[:n])
    return list(out[n:])


def _tie(x, token, name):
    def body(x_ref, t_ref, o_ref):
        pass

    return pl.pallas_call(
        body, name=name, out_shape=jax.ShapeDtypeStruct(x.shape, x.dtype),
        in_specs=[ANY, ANY], out_specs=ANY, input_output_aliases={0: 0},
    )(x, token)


def _forward_diagonal(land, name, rows):
    mid = (rows[0] + rows[1]) // 2
    half_a, half_b = pl.ds(rows[0], mid - rows[0]), pl.ds(mid, rows[1] - mid)

    def body(land_ref, out_ref, send_sems, recv_sems):
        x, y, c = _place()
        diag = _slot((1 - x, 1 - y, c))

        def copy(blk, part, j, to):
            return pltpu.make_async_remote_copy(src_ref=land_ref.at[blk, part], dst_ref=out_ref.at[blk, part],
                                                send_sem=send_sems.at[j], recv_sem=recv_sems.at[j],
                                                device_id=to, device_id_type=MESH)

        sends = [copy(_slot((1 - x, y, c)), half_a, 0, (x, 1 - y, c)),
                 copy(_slot((x, 1 - y, c)), half_b, 1, (1 - x, y, c))]
        for cp in sends:
            cp.start()
        copy(diag, half_a, 0, (x, 1 - y, c)).wait_recv()
        copy(diag, half_b, 1, (1 - x, y, c)).wait_recv()
        for cp in sends:
            cp.wait_send()

    return pl.pallas_call(
        body, name=name, in_specs=[ANY], out_specs=ANY, input_output_aliases={0: 0},
        out_shape=jax.ShapeDtypeStruct(land.shape, land.dtype),
        scratch_shapes=[pltpu.SemaphoreType.DMA((2,)), pltpu.SemaphoreType.DMA((2,))],
    )(land)


def _forward_to_sibling(land, name, rows):
    def body(land_ref, out_ref, send_sems, recv_sems):
        x, y, c = _place()
        sibling = (x, y, 1 - c)
        chips = [(1 - x, y), (x, 1 - y), (1 - x, 1 - y)]
        piece = pl.ds(rows[0], rows[1] - rows[0])

        def copy(j, core):
            blk = _slot((*chips[j], core))
            return pltpu.make_async_remote_copy(src_ref=land_ref.at[blk, piece], dst_ref=out_ref.at[blk, piece],
                                                send_sem=send_sems.at[j], recv_sem=recv_sems.at[j],
                                                device_id=sibling, device_id_type=MESH)

        sends = [copy(j, c) for j in range(3)]
        for cp in sends:
            cp.start()
        for j in range(3):
            copy(j, 1 - c).wait_recv()
        for cp in sends:
            cp.wait_send()

    return pl.pallas_call(
        body, name=name, in_specs=[ANY], out_specs=ANY, input_output_aliases={0: 0},
        out_shape=jax.ShapeDtypeStruct(land.shape, land.dtype),
        scratch_shapes=[pltpu.SemaphoreType.DMA((3,)), pltpu.SemaphoreType.DMA((3,))],
    )(land)


def _mod_fwd(a, w, b):
    def body(a_ref, w_ref, b_ref, o_ref):
        o_ref[...] = _dot(_silu(a_ref[...]), w_ref[...], NN, precision=HI) + b_ref[...]

    return pl.pallas_call(
        body, name="mod_fwd", out_shape=jax.ShapeDtypeStruct((a.shape[0], w.shape[1]), F32),
        compiler_params=pltpu.CompilerParams(vmem_limit_bytes=VMEM_LIMIT),
    )(a, w, b)


def _mod_bwd(a, d, w):
    def body(a_ref, d_ref, w_ref, dw_ref, dc_ref):
        av = a_ref[...]
        dv = d_ref[...]
        dw_ref[...] = _dot(_silu(av), dv, TN, precision=HI)
        da = _dot(dv[0:8, :], w_ref[...], NT, precision=HI) * _dsilu(av[0:8, :])
        row = lax.broadcasted_iota(jnp.int32, da.shape, 0)
        dc_ref[...] = jnp.where(row == 0, da, 0.0)

    return pl.pallas_call(
        body, name="mod_bwd",
        out_shape=[jax.ShapeDtypeStruct(w.shape, F32), jax.ShapeDtypeStruct((8, w.shape[0]), F32)],
        compiler_params=pltpu.CompilerParams(vmem_limit_bytes=VMEM_LIMIT),
    )(a, d, w)


def _sum_devices(g):
    def body(g_ref, o_ref):
        acc = g_ref[0]
        for i in range(1, g.shape[0]):
            acc = acc + g_ref[i]
        o_ref[...] = acc

    return pl.pallas_call(body, name="sum_devices_%d" % g.shape[1],
                          out_shape=jax.ShapeDtypeStruct(g.shape[1:], F32))(g)


def _sum_windows(g, name):
    n, r, c = g.shape
    tr = 128

    def body(g_ref, o_ref):
        x, y, cc = _place()
        lane0 = (W_IN_SHARD * _slot((x, y, cc))) % DH
        acc = g_ref[0].astype(F32)
        for i in range(1, n):
            acc = acc + g_ref[i].astype(F32)
        o_ref[...] = pltpu.roll(acc, (c - lane0) % c, 1).T

    return pl.pallas_call(
        body, name=name, grid=(r // tr,),
        in_specs=[pl.BlockSpec((n, tr, c), lambda i: (0, i, 0))],
        out_specs=pl.BlockSpec((c, tr), lambda i: (0, i)),
        out_shape=jax.ShapeDtypeStruct((c, r), F32),
        compiler_params=_cp("parallel"),
    )(g)


def _adam_rows(r, c, n):
    budget = 10 * 1024 * 1024
    best = None
    for tr in range(16, r + 1, 16):
        if r % tr == 0 and tr * c * (2 * n + 28) <= budget:
            best = tr
    return best if best is not None else r


def _adamw(g, w, m, v, name):
    n, r, c = g.shape
    tr = _adam_rows(r, c, n)
    bc1 = 1.0 - ADAM_B1 ** ADAM_STEP
    bc2 = 1.0 - ADAM_B2 ** ADAM_STEP

    def body(g_ref, w_ref, m_ref, v_ref, go_ref, d_ref, mo_ref, vo_ref):
        grad = g_ref[0].astype(F32)
        for i in range(1, n):
            grad = grad + g_ref[i].astype(F32)
        go_ref[...] = grad
        m_new = ADAM_B1 * m_ref[...] + (1.0 - ADAM_B1) * grad
        v_new = ADAM_B2 * v_ref[...] + (1.0 - ADAM_B2) * (grad * grad)
        mo_ref[...] = m_new
        vo_ref[...] = v_new
        d_ref[...] = -ADAM_LR * ((m_new / bc1) / (jnp.sqrt(v_new / bc2) + ADAM_EPS) + ADAM_WD * w_ref[...])

    blk = pl.BlockSpec((tr, c), lambda i: (i, 0))
    out = jax.ShapeDtypeStruct((r, c), F32)
    return pl.pallas_call(
        body, name=name, grid=(r // tr,),
        in_specs=[pl.BlockSpec((n, tr, c), lambda i: (0, i, 0)), blk, blk, blk],
        out_specs=[blk] * 4, out_shape=[out] * 4,
        compiler_params=_cp("parallel"),
    )(g, w, m, v)


ADAM_ROWS3 = 168


def _adam_math(grad, w, m, v):
    bc1 = 1.0 - ADAM_B1 ** ADAM_STEP
    bc2 = 1.0 - ADAM_B2 ** ADAM_STEP
    m_new = ADAM_B1 * m + (1.0 - ADAM_B1) * grad
    v_new = ADAM_B2 * v + (1.0 - ADAM_B2) * (grad * grad)
    delta = -ADAM_LR * ((m_new / bc1) / (jnp.sqrt(v_new / bc2) + ADAM_EPS) + ADAM_WD * w)
    return delta, m_new, v_new


def _adamw_rows3(g, w3, m3, v3, name, cols, prev):
    r, _, _ = w3.shape
    c = cols[1] - cols[0]
    n = min(-(-r // 16) * 8, ADAM_ROWS3 * D // c // 8 * 8)
    starts = list(range(0, r - n, n)) + [r - n]
    held = [] if prev is None else list(prev)

    def body(g_hbm, w_hbm, m_hbm, v_hbm, *refs):
        go_hbm, d_hbm, mo_hbm, vo_hbm, gbuf, ibuf, obuf, in_sems, out_sems = refs[len(held):]
        part = lambda h, r0: h.at[pl.ds(r0, n), 0, pl.ds(cols[0], c)]

        def fetch(p):
            r0, slot = starts[p], p % 2
            g0 = (r0 // 8) * 8
            cps = [pltpu.make_async_copy(g_hbm.at[pl.ds(g0, n + 8)], gbuf.at[slot], in_sems.at[slot, 0])]
            cps += [pltpu.make_async_copy(part(h, r0), ibuf.at[slot, k], in_sems.at[slot, 1 + k])
                    for k, h in enumerate((w_hbm, m_hbm, v_hbm))]
            for cp in cps:
                cp.start()
            return cps

        pending, outs = fetch(0), []
        for p, r0 in enumerate(starts):
            slot = p % 2
            nxt = fetch(p + 1) if p + 1 < len(starts) else []
            for cp in pending:
                cp.wait()
            grad = gbuf[slot, pl.ds(r0 - (r0 // 8) * 8, n), :]
            delta, m_new, v_new = _adam_math(grad, ibuf[slot, 0], ibuf[slot, 1], ibuf[slot, 2])
            for cp in outs:
                cp.wait()
            for k, val in enumerate((grad, delta, m_new, v_new)):
                obuf[slot, k] = val
            outs = [pltpu.make_async_copy(obuf.at[slot, k], part(h, r0), out_sems.at[slot, k])
                    for k, h in enumerate((go_hbm, d_hbm, mo_hbm, vo_hbm))]
            for cp in outs:
                cp.start()
            pending = nxt
        for cp in outs:
            cp.wait()

    out = jax.ShapeDtypeStruct(w3.shape, F32)
    return pl.pallas_call(
        body, name=name, in_specs=[ANY] * (4 + len(held)), out_specs=[ANY] * 4, out_shape=[out] * 4,
        input_output_aliases={4 + k: k for k in range(len(held))},
        scratch_shapes=[pltpu.VMEM((2, n + 8, c), F32), pltpu.VMEM((2, 3, n, c), F32), pltpu.VMEM((2, 4, n, c), F32),
                        pltpu.SemaphoreType.DMA((2, 4)), pltpu.SemaphoreType.DMA((2, 4))],
        compiler_params=pltpu.CompilerParams(vmem_limit_bytes=VMEM_LIMIT),
    )(g, w3, m3, v3, *held)


def kernel(x, c, ctx, c_ctx, w_mod, b_mod, norm_pre1, norm_post1, norm_pre2, norm_post2, w_in, hg_lb, hg_onorm, gla_w_gk, gla_b_gk, gla_onorm, w_br_hg, w_br_gla, w_out, w_ff_gate, w_ff_up, w_ff_down, loss_target, m_c_ctx, m_w_mod, m_b_mod, m_norm_pre1, m_norm_post1, m_norm_pre2, m_norm_post2, m_w_in, m_hg_lb, m_hg_onorm, m_gla_w_gk, m_gla_b_gk, m_gla_onorm, m_w_br_hg, m_w_br_gla, m_w_out, m_w_ff_gate, m_w_ff_up, m_w_ff_down, v_c_ctx, v_w_mod, v_b_mod, v_norm_pre1, v_norm_post1, v_norm_pre2, v_norm_post2, v_w_in, v_hg_lb, v_hg_onorm, v_gla_w_gk, v_gla_b_gk, v_gla_onorm, v_w_br_hg, v_w_br_gla, v_w_out, v_w_ff_gate, v_w_ff_up, v_w_ff_down):
    xi, yi, ci = lax.axis_index("x"), lax.axis_index("y"), lax.axis_index("c")
    me = 4 * xi + 2 * yi + ci
    t = CTX + x.shape[1]

    w_in_pieces, w_in_state = [], {}

    def w_in_piece(i):
        return (_view_near_rows((i * W_IN_PIECE, (i + 1) * W_IN_PIECE)), w_in_state["src"], w_in_state["land"])

    def started_w_in(handle):
        w_in_state.update(src=handle["srcs"], land=handle["lands"])
        w_in_pieces.append(handle)

    tr_ = lambda a: jnp.swapaxes(a[0], 0, 1)
    w_in_bf = jnp.pad(w_in[0].astype(BF16), ((0, 0), (0, W_IN_PAD - W_IN_SHARD)))
    w_in_state.update(src=[w_in_bf], land=[lax.empty((N_DEV,) + w_in_bf.shape, BF16)])
    gathered = lambda arrs: [(N_DEV,) + a.shape for a in arrs]
    whole = lambda arrs: (_view_whole, arrs, gathered(arrs))
    small_in = [c, hg_lb, gla_w_gk[0], gla_b_gk[0]]
    (small_handle, piece), tok = _split_start([whole(small_in), w_in_piece(0)], "ag_small_start", c)
    started_w_in(piece)
    c_all, lb_g, wgk_g, bgk_g = _split_wait(small_handle, "ag_small_wait", tok)
    big = [w_in[0], w_br_hg[0], w_br_gla[0], w_out[0], tr_(w_ff_gate), tr_(w_ff_up), w_ff_down[0]]
    big_bf = [None] + [w.astype(BF16) for w in big[1:]]
    cols = lambda g: jnp.transpose(g, (1, 0, 2)).reshape(g.shape[1], N_DEV * g.shape[2])

    def get_w_in(after):
        w_full, first = None, 0
        for s, last in enumerate(W_IN_STAGES):
            for i in range(first, last):
                land = _split_wait(w_in_pieces[i], "ag_w_in_wait%d" % i, after if w_full is None else [after, w_full],
                                   srcs=w_in_state["src"], lands=w_in_state["land"])
                w_in_state.update(src=w_in_pieces[i]["srcs"], land=land)
            rows = (first * W_IN_PIECE, last * W_IN_PIECE)
            crossed = _forward_diagonal(w_in_state["land"][0], "ag_w_in_diagonal%d" % s, rows)
            w_in_state["land"] = [_forward_to_sibling(crossed, "ag_w_in_forward%d" % s, rows)]
            w_full = _assemble_w_in(w_in_state["land"][0], rows, w_full, "assemble_w_in%d" % s)
            first = last
        return w_full

    def get_mix(after):
        g_brh, g_brg, g_out = _split_wait(mix_handle, "ag_mix_wait", after)
        return _gate_cols(cols(g_brh)), _gate_cols(cols(g_brg)), _gate_rows(g_out.reshape(D, D))

    def get_ffn(after):
        g_gate, g_up = _split_wait(ffn_handle, "ag_ffn_wait", after)

        def get_down(after):
            g_down, = _split_wait(down_handle, "ag_down_wait", after)
            return g_down.reshape(D_FF, D)

        return (g_gate.reshape(D_FF, D), g_up.reshape(D_FF, D)), get_down

    hg_lb_full = jnp.transpose(lb_g, (1, 2, 0, 3)).reshape(2, 2, HW)
    wgk_k = _layout_wgk(jnp.transpose(wgk_g, (1, 2, 0, 3)).reshape(2, 16, HW)).astype(BF16)
    bgk_k = jnp.transpose(bgk_g, (1, 0, 2)).reshape(1, D)
    onw = jnp.concatenate([jnp.tile(hg_onorm, (1, NH // 2)), jnp.tile(gla_onorm, (1, NH // 2))], axis=1)

    n_mod = w_mod.shape[2]
    a9 = jnp.concatenate([c_ctx[None], c_all[:, 0], jnp.zeros((16 - 1 - N_DEV, D), F32)], axis=0)
    b_loc = lax.dynamic_slice(b_mod, (0, me * n_mod), (1, n_mod))
    s_loc = _mod_fwd(a9, w_mod[0], b_loc)
    (mod_handle, piece), tok = _split_start([whole([s_loc]), w_in_piece(1)], "ag_mod_start", s_loc)
    started_w_in(piece)
    for i in range(2, D // W_IN_PIECE):
        (piece,), tok = _split_start([w_in_piece(i)], "ag_w_in_start%d" % i, tok)
        started_w_in(piece)
    s_all, = _split_wait(mod_handle, "ag_mod_wait", tok)
    mod_all = jnp.transpose(s_all, (1, 0, 2)).reshape(16, N_DEV * n_mod)
    pad8 = lambda m: jnp.concatenate([m.reshape(6, D), jnp.zeros((2, D), F32)], axis=0)
    modc = pad8(mod_all[0])
    modx = pad8(lax.dynamic_slice(mod_all, (1 + me, 0), (1, N_DEV * n_mod))[0])

    (mix_handle, ffn_handle, down_handle), tok = _split_start(
        [whole(big_bf[1:4]), whole(big_bf[4:6]), whole(big_bf[6:])], "ag_big_start", s_all)

    z = (ctx[0], x[0])
    modx = _tie(modx, tok, "tie_mod")
    norms = (norm_pre1, norm_post1, norm_pre2, norm_post2)
    rowshard = lambda d: d.reshape(N_DEV, d.shape[0] // N_DEV, d.shape[1]).astype(BF16)
    sent, w_in_grad = [], {}

    def w_in_chunk(i):
        half, rows = W_IN_GRAD_CHUNKS[i]
        return (_view_window(rows), w_in_grad[half], [(N_DEV, rows[1] - rows[0], D)])

    def sent_w_in(i, handle):
        w_in_grad[W_IN_GRAD_CHUNKS[i][0]] = handle["srcs"]
        sent.append(("w_in%d" % i, ["w_in#%d" % i], handle))

    def send(names, grads, x_after):
        if names == ("w_in_a",):
            w_in_grad["a"] = list(grads)
            (handle,), tok = _split_start([w_in_chunk(0)], "grads_w_in0_start", x_after)
            sent_w_in(0, handle)
            return _tie(x_after, tok, "tie_w_in0")
        if names == ("w_in_b",):
            w_in_grad["b"] = list(grads)
            return x_after
        arrs, leaves, col_arrs, col_leaves = [], [], [], []
        for nm, g in zip(names, grads):
            if nm in ("w_gate_t", "w_up_t"):
                arrs.append(rowshard(g))
                leaves.append({"w_gate_t": "w_ff_gate", "w_up_t": "w_ff_up"}[nm])
            elif nm == "w_down":
                arrs.append(rowshard(g))
                leaves.append("w_ff_down")
            elif nm == "w_out":
                arrs.append(rowshard(g[GOFF:GOFF + D]))
                leaves.append(nm)
            else:
                col_arrs.append(g[:, GOFF:GOFF + D])
                col_leaves.append(nm)
        groups = [(_view_block, arrs, [a.shape for a in arrs])]
        if col_arrs:
            groups.append((_view_cols, col_arrs, [(N_DEV, a.shape[0], D // N_DEV) for a in col_arrs]))
        handles, tok = _split_start(groups, "grads_%s_start" % names[0], x_after)
        sent.append((names[0], leaves, handles[0]))
        if col_arrs:
            sent.append((names[0] + "_cols", col_leaves, handles[1]))
        return _tie(x_after, tok, "tie_" + names[0])

    r = _local_step(z, loss_target[0], modc, modx, norms, onw, hg_lb_full, wgk_k, bgk_k,
                    get_w_in, get_mix, get_ffn, send)
    grad_x = r["grad_x"][None]

    sm_pre, sm_mid, sm_fin = r["sm_pre"], r["sm_mid"], r["sm_final"]
    dmodc = jnp.stack([sm_pre[0], sm_pre[2], sm_mid[4], sm_mid[0], sm_mid[2], sm_fin[0]]).reshape(-1)
    dmodx = jnp.stack([sm_pre[1], sm_pre[3], sm_mid[5], sm_mid[1], sm_mid[3], sm_fin[1]]).reshape(-1)
    on = r["sm_post"][0].reshape(NH, DH)
    pieces = [dmodc, dmodx, sm_pre[4], sm_mid[7], sm_mid[6], sm_fin[2], on[:NH // 2].sum(0), on[NH // 2:].sum(0),
              r["d_lb"][:2].reshape(-1), _unlayout_wgk(r["d_wgk"]).reshape(-1), r["d_bgk"][0]]
    loss_local = (0.5 / D) * jnp.sum(r["loss_vec"])
    pieces.append(jnp.concatenate([loss_local.reshape(1), jnp.zeros((DH - 1,), F32)]))
    sizes = [p.shape[0] for p in pieces]
    pack = jnp.concatenate(pieces).reshape(-1, DH)
    moms = [(m_w_in, v_w_in), (m_w_br_hg, v_w_br_hg), (m_w_br_gla, v_w_br_gla), (m_w_out, v_w_out),
            (m_w_ff_gate, v_w_ff_gate), (m_w_ff_up, v_w_ff_up), (m_w_ff_down, v_w_ff_down)]
    names = ["w_in", "w_br_hg", "w_br_gla", "w_out", "w_ff_gate", "w_ff_up", "w_ff_down"]
    wmv = {nm: (w, m, v) for nm, w, (m, v) in zip(names, big, moms)}
    res, updated = {}, {}

    def update(nm):
        w, m, v = wmv[nm]
        if nm in ("w_ff_gate", "w_ff_up"):
            outs = _adamw(recv[nm], w, tr_(m), tr_(v), "adamw_" + nm)
            res[nm] = [jnp.swapaxes(o, 0, 1)[None] for o in outs]
        else:
            outs = _adamw(recv[nm], w, m[0], v[0], "adamw_" + nm)
            res[nm] = [o[None] for o in outs]
        updated[nm] = outs[0]

    (small_handle, handle), tok = _split_start([whole([pack]), w_in_chunk(1)], "small_grads_start", pack)
    sent_w_in(1, handle)
    recv = {}
    for first, leaves, handle in sent:
        if not first.startswith("w_in"):
            recv.update(zip(leaves, _split_wait(handle, "grads_%s_wait" % first, tok)))
    update("w_ff_gate")
    update("w_ff_up")
    pack_all, = _split_wait(small_handle, "small_grads_wait", [updated["w_ff_gate"], updated["w_ff_up"]])
    tot = _sum_devices(pack_all).reshape(-1)
    offs = [sum(sizes[:i]) for i in range(len(sizes))]
    part = lambda i: tot[offs[i]:offs[i] + sizes[i]]
    dmodc_t, dmodx_t = part(0), part(1)
    g_b_mod = (dmodc_t + dmodx_t)[None]
    g_norms = [part(i)[None] for i in (2, 3, 4, 5)]
    g_hg_on, g_gla_on = part(6)[None], part(7)[None]
    lb0 = lax.dynamic_slice(part(8).reshape(2, HW), (0, me * (HW // N_DEV)), (2, HW // N_DEV))
    g_hg_lb = jnp.stack([lb0, -lb0])
    g_wgk = lax.dynamic_slice(part(9).reshape(2, 16, HW), (0, 0, me * (HW // N_DEV)), (2, 16, HW // N_DEV))[None]
    g_bgk = lax.dynamic_slice(part(10).reshape(2, HW), (0, me * (HW // N_DEV)), (2, HW // N_DEV))[None]
    loss = part(11)[0]

    dmx_all = pack_all.reshape(N_DEV, -1)[:, sizes[0]:sizes[0] + sizes[1]]
    d9 = jnp.concatenate([lax.dynamic_slice(dmodc_t[None], (0, me * n_mod), (1, n_mod)),
                          lax.dynamic_slice(dmx_all, (0, me * n_mod), (N_DEV, n_mod)),
                          jnp.zeros((16 - 1 - N_DEV, n_mod), F32)], axis=0)
    g_w_mod, dcc_part = _mod_bwd(a9, d9, w_mod[0])
    (cctx_handle, handle), tok = _split_start([whole([dcc_part]), w_in_chunk(2)], "c_ctx_start", dcc_part)
    sent_w_in(2, handle)
    recv["w_ff_down"] = _tie(recv["w_ff_down"], tok, "tie_down")
    update("w_ff_down")
    res["w_mod"] = [o[None] for o in _adamw(g_w_mod[None], w_mod[0], m_w_mod[0], v_w_mod[0], "adamw_w_mod")]
    for nm in ("w_out", "w_br_hg", "w_br_gla"):
        update(nm)
    dcc_all, = _split_wait(cctx_handle, "c_ctx_wait", [updated[nm] for nm in names[1:]] + [res["w_mod"][0]])
    g_c_ctx = _sum_devices(dcc_all)[0]

    small = [("c_ctx", c_ctx, m_c_ctx, v_c_ctx, g_c_ctx), ("b_mod", b_mod, m_b_mod, v_b_mod, g_b_mod),
             ("norm_pre1", norm_pre1, m_norm_pre1, v_norm_pre1, g_norms[0]),
             ("norm_post1", norm_post1, m_norm_post1, v_norm_post1, g_norms[1]),
             ("norm_pre2", norm_pre2, m_norm_pre2, v_norm_pre2, g_norms[2]),
             ("norm_post2", norm_post2, m_norm_post2, v_norm_post2, g_norms[3]),
             ("hg_lb", hg_lb, m_hg_lb, v_hg_lb, g_hg_lb), ("hg_onorm", hg_onorm, m_hg_onorm, v_hg_onorm, g_hg_on),
             ("gla_w_gk", gla_w_gk, m_gla_w_gk, v_gla_w_gk, g_wgk), ("gla_b_gk", gla_b_gk, m_gla_b_gk, v_gla_b_gk, g_bgk),
             ("gla_onorm", gla_onorm, m_gla_onorm, v_gla_onorm, g_gla_on)]
    flat = lambda k: jnp.concatenate([s[k].reshape(-1) for s in small]).reshape(-1, DH)
    outs = _adamw(flat(4)[None], flat(1), flat(2), flat(3), "adamw_small")
    off = 0
    for nm, w, _, _, _ in small:
        res[nm] = [o.reshape(-1)[off:off + w.size].reshape(w.shape) for o in outs]
        off += w.size

    done = [updated[nm] for nm in names[1:]] + [res["w_mod"][0]] + [o for nm, *_ in small for o in res[nm]]
    major = lambda a: jnp.transpose(a, (2, 0, 1))
    outs, row0 = None, 0
    for i, (first, leaves, handle) in enumerate(s for s in sent if s[0].startswith("w_in")):
        half = W_IN_GRAD_CHUNKS[i][0]
        land, = _split_wait(handle, "grads_%s_wait" % first, done, srcs=w_in_grad[half])
        w_in_grad[half] = handle["srcs"]
        rows = (row0, row0 + land.shape[1])
        outs = _adamw_rows3(_sum_windows(land, "sum_windows%d" % i), major(w_in), major(m_w_in), major(v_w_in),
                            "adamw_w_in%d" % i, rows, outs)
        row0 = rows[1]
    res["w_in"] = [jnp.transpose(o, (1, 2, 0)) for o in outs]

    order = ["c_ctx", "w_mod", "b_mod", "norm_pre1", "norm_post1", "norm_pre2", "norm_post2", "w_in", "hg_lb",
             "hg_onorm", "gla_w_gk", "gla_b_gk", "gla_onorm", "w_br_hg", "w_br_gla", "w_out", "w_ff_gate", "w_ff_up",
             "w_ff_down"]
    return (loss, grad_x, *[res[n][k] for k in range(4) for n in order])
```

```python
import functools

import jax
import jax.numpy as jnp
from jax import lax
from jax.experimental import pallas as pl
from jax.experimental.pallas import tpu as pltpu

F32 = jnp.float32
BF16 = jnp.bfloat16
HI = lax.Precision.HIGHEST

N_DEV = 8
D = 1024
CTX = 256
HW = 512
DH = 128
NH = 8
D_FF = 2816
EPS = 1e-6
GLA_NORM = 16.0
CHUNK = 64
TR = 256
NCT = CTX // TR
W_IN_COLS = 7168
MAIN0 = 0
LR0 = 4608
GW = 1152
GOFF = 32
GATE_HG0 = LR0
GATE_GLA0 = LR0 + D
LEVELS = (32, 16, 8)
EXP_CLAMP = 80.0
VMEM_LIMIT = 48 * 1024 * 1024

ADAM_LR, ADAM_B1, ADAM_B2, ADAM_EPS, ADAM_WD, ADAM_STEP = 0.001, 0.9, 0.999, 1e-08, 0.01, 10


def _cp(*sem):
    return pltpu.CompilerParams(dimension_semantics=sem, vmem_limit_bytes=VMEM_LIMIT)


def _sig(x):
    return jax.nn.sigmoid(x)


def _silu(x):
    return x * _sig(x)


def _dsilu(x):
    s = _sig(x)
    return s * (1.0 + x * (1.0 - s))


def _rstd(x):
    return lax.rsqrt(jnp.mean(x * x, axis=-1, keepdims=True) + EPS)


def _rms_bwd(a, y, r):
    return r * (a - y * (r * r) * jnp.mean(a * y, axis=-1, keepdims=True))


def _colsum(x):
    return jnp.sum(x, axis=0, keepdims=True)


def _dot(a, b, dims, precision=None):
    return lax.dot_general(a, b, (dims, ((), ())), preferred_element_type=F32, precision=precision)


NN = ((1,), (0,))
NT = ((1,), (1,))
TN = ((0,), (0,))

SCAN_HEADS_FWD = 4
SCAN_HEADS_BWD = 4


def _split_dot(m, x):
    mb = m.astype(BF16)
    x1 = x.astype(BF16)
    r1 = x - x1.astype(F32)
    x2 = r1.astype(BF16)
    x3 = (r1 - x2.astype(F32)).astype(BF16)
    return _dot(mb, x1, NN) + _dot(mb, x2, NN) + _dot(mb, x3, NN)


def _matmul(a, b, dims, out_dtype, name, tm, tn, tk, a_off=0, m_out=None):
    a_pair = isinstance(a, (tuple, list))
    as_ = list(a) if a_pair else [a]
    a = as_[0]
    pair = isinstance(b, (tuple, list))
    bs = list(b) if pair else [b]
    b1 = bs[0]
    rows = b1.shape[0] * len(bs)
    half = None
    if dims == NN:
        m, k, n = a.shape[0], rows, b1.shape[1]
        a_spec = pl.BlockSpec((tm, tk), lambda i, j, kk: (i, kk + a_off))
        half = b1.shape[0] // tk
        if a_pair:
            assert pair and a.shape[1] == b1.shape[0] and a_off == 0
            a_spec = [pl.BlockSpec((tm, tk), lambda i, j, kk: (i, jnp.minimum(kk, half - 1))),
                      pl.BlockSpec((tm, tk), lambda i, j, kk: (i, jnp.maximum(kk - half, 0)))]
        b_maps = [lambda i, j, kk: (kk, j)] if not pair else [
            lambda i, j, kk: (jnp.minimum(kk, half - 1), j), lambda i, j, kk: (jnp.maximum(kk - half, 0), j)]
        b_specs = [pl.BlockSpec((tk, tn), f) for f in b_maps]
        axis = 2
    elif dims == NT:
        m, k, n = a.shape[0], b1.shape[1], rows
        a_spec = pl.BlockSpec((tm, tk), lambda i, j, kk: (i, kk + a_off))
        half = b1.shape[0] // tn
        b_maps = [lambda i, j, kk: (j, kk)] if not pair else [
            lambda i, j, kk: (jnp.minimum(j, half - 1), kk), lambda i, j, kk: (jnp.maximum(j - half, 0), kk)]
        b_specs = [pl.BlockSpec((tn, tk), f) for f in b_maps]
        axis = 1
    else:
        assert not pair
        m, k = (a.shape[1] if m_out is None else m_out), a.shape[0]
        n = b1.shape[1]
        a_spec = pl.BlockSpec((tk, tm), lambda i, j, kk: (kk, i + a_off))
        b_specs = [pl.BlockSpec((tk, tn), lambda i, j, kk: (kk, j))]
    assert m % tm == 0 and n % tn == 0 and k % tk == 0, (name, m, n, k, tm, tn, tk)
    nk = k // tk
    nb = len(bs)
    na = len(as_)
    assert na == 1 or dims == NN

    def body(*refs):
        a_refs, refs = refs[:na], refs[na:]
        o_ref = refs[nb]
        if pair:
            bv = jnp.where(pl.program_id(axis) < half, refs[0][...], refs[1][...])
        else:
            bv = refs[0][...]
        av = a_refs[0][...] if na == 1 else jnp.where(pl.program_id(2) < half, a_refs[0][...], a_refs[1][...])
        part = _dot(av, bv, dims)
        if nk == 1:
            o_ref[...] = part.astype(o_ref.dtype)
            return
        acc_ref = refs[nb + 1]
        kk = pl.program_id(2)

        @pl.when(kk == 0)
        def _():
            acc_ref[...] = part

        @pl.when(kk > 0)
        def _():
            acc_ref[...] += part

        @pl.when(kk == nk - 1)
        def _():
            o_ref[...] = acc_ref[...].astype(o_ref.dtype)

    return pl.pallas_call(
        body,
        name=name,
        grid=(m // tm, n // tn, nk),
        in_specs=(a_spec if a_pair else [a_spec]) + b_specs,
        out_specs=pl.BlockSpec((tm, tn), lambda i, j, kk: (i, j)),
        out_shape=jax.ShapeDtypeStruct((m, n), out_dtype),
        scratch_shapes=[] if nk == 1 else [pltpu.VMEM((tm, tn), F32)],
        compiler_params=_cp("parallel", "parallel", "arbitrary"),
    )(*as_, *bs)


def _mm_gu_act(h, w_gate_t, w_up_t, name, tm):
    t = h.shape[0]
    tn = D_FF // 2

    def body(a_ref, bg_ref, bu_ref, u_ref, v_ref, act_ref):
        a = a_ref[...]
        u = _dot(a, bg_ref[...], NT)
        v = _dot(a, bu_ref[...], NT)
        u_ref[...] = u.astype(BF16)
        v_ref[...] = v.astype(BF16)
        act_ref[...] = (_silu(u) * v).astype(BF16)

    wspec = pl.BlockSpec((tn, D), lambda i, j: (j, 0))
    ospec = pl.BlockSpec((tm, tn), lambda i, j: (i, j))
    out = jax.ShapeDtypeStruct((t, D_FF), BF16)
    return pl.pallas_call(
        body, name=name, grid=(t // tm, D_FF // tn),
        in_specs=[pl.BlockSpec((tm, D), lambda i, j: (i, 0)), wspec, wspec],
        out_specs=[ospec] * 3, out_shape=[out] * 3,
        compiler_params=_cp("parallel", "parallel"),
    )(h, w_gate_t, w_up_t)


def _mm_down_dx_act(dy, w_down, u, v, name, tm):
    t = dy.shape[0]
    tn = D_FF // 2

    def body(a_ref, b_ref, u_ref, v_ref, du_ref, dv_ref):
        dact = _dot(a_ref[...], b_ref[...], NT)
        u = u_ref[...].astype(F32)
        du_ref[...] = (dact * v_ref[...].astype(F32) * _dsilu(u)).astype(BF16)
        dv_ref[...] = (dact * _silu(u)).astype(BF16)

    ospec = pl.BlockSpec((tm, tn), lambda i, j: (i, j))
    out = jax.ShapeDtypeStruct((t, D_FF), BF16)
    return pl.pallas_call(
        body, name=name, grid=(t // tm, D_FF // tn),
        in_specs=[pl.BlockSpec((tm, D), lambda i, j: (i, 0)), pl.BlockSpec((tn, D), lambda i, j: (j, 0)), ospec, ospec],
        out_specs=[ospec] * 2, out_shape=[out] * 2,
        compiler_params=_cp("parallel", "parallel"),
    )(dy, w_down, u, v)


def _row(c):
    return pl.BlockSpec((TR, c), lambda i: (i, 0))


def _rowcol(width, cb):
    return pl.BlockSpec((TR, width), lambda i: (i, cb))


def _full(shape):
    return pl.BlockSpec(shape, lambda i: (0,) * len(shape))


def _mod_row(mc_ref, mx_ref, k, is_ctx):
    return jnp.where(is_ctx, mc_ref[k:k + 1, :], mx_ref[k:k + 1, :])


def _z_specs():
    return [pl.BlockSpec((TR, D), lambda i: (jnp.minimum(i, NCT - 1), 0)),
            pl.BlockSpec((TR, D), lambda i: (jnp.maximum(i - NCT, 0), 0))]


def _z_tile(c_ref, x_ref, is_ctx):
    return jnp.where(is_ctx, c_ref[...], x_ref[...])


def _acc_row(ref, k, val):
    ref[k:k + 1, :] += val


def _acc_mod(ref, k, is_ctx, val):
    zero = jnp.zeros_like(val)
    ref[k:k + 1, :] += jnp.where(is_ctx, val, zero)
    ref[k + 1:k + 2, :] += jnp.where(is_ctx, zero, val)


def _prenorm(z, nw, modc, modx, i_shift, i_scale, name):
    t = z[0].shape[0] + z[1].shape[0]

    def body(zc_ref, zx_ref, nw_ref, mc_ref, mx_ref, h_ref):
        is_ctx = pl.program_id(0) < NCT
        x = _z_tile(zc_ref, zx_ref, is_ctx)
        n = x * _rstd(x) * nw_ref[...]
        h = n * (1.0 + _mod_row(mc_ref, mx_ref, i_scale, is_ctx)) + _mod_row(mc_ref, mx_ref, i_shift, is_ctx)
        h_ref[...] = h.astype(BF16)

    return pl.pallas_call(
        body, name=name, grid=(t // TR,),
        in_specs=_z_specs() + [_full((1, D)), _full((8, D)), _full((8, D))],
        out_specs=_row(D),
        out_shape=jax.ShapeDtypeStruct((t, D), BF16),
        compiler_params=_cp("parallel"),
    )(*z, nw, modc, modx)


def _hg_lb(lb_ref, d):
    a0 = lb_ref[0, d:d + 1, :]
    a1 = lb_ref[1, d:d + 1, :]
    mx = jnp.maximum(a0, a1)
    e0 = jnp.exp(a0 - mx)
    e1 = jnp.exp(a1 - mx)
    return e0 / (e0 + e1)


def _log_sigmoid(x):
    return jnp.minimum(x, 0.0) - jnp.log(1.0 + jnp.exp(-jnp.abs(x)))


def _gates_fwd(p, hg_lb, wgk, bgk):
    t = p.shape[0]
    seg = lambda j: _rowcol(HW, MAIN0 // HW + j)

    def body(hq_ref, hi_ref, hf_ref, hb_ref, gq_ref, gk_ref, gv_ref, lr_ref, lb_ref, wgk_ref, bgk_ref,
             q_ref, v_ref, kf_ref, kb_ref, gf_ref, gb_ref):
        q_ref[:, :HW] = _silu(hq_ref[...].astype(F32)).astype(BF16)
        q_ref[:, HW:] = (gq_ref[...].astype(F32) * (DH ** -0.5)).astype(BF16)
        v_ref[:, :HW] = hi_ref[...]
        v_ref[:, HW:] = gv_ref[...]
        xg = _dot(lr_ref[...].astype(BF16), wgk_ref[...], NN) + bgk_ref[...]
        for d, (raw_ref, k_ref, g_ref) in enumerate(((hf_ref, kf_ref, gf_ref), (hb_ref, kb_ref, gb_ref))):
            lbd = _hg_lb(lb_ref, d)
            f = lbd + (1.0 - lbd) * _sig(raw_ref[...].astype(F32))
            k_ref[:, :HW] = (1.0 - f).astype(BF16)
            k_ref[:, HW:] = gk_ref[...]
            g_ref[:, :HW] = jnp.log(f)
            g_ref[:, HW:] = _log_sigmoid(xg[:, d * HW:(d + 1) * HW]) * (1.0 / GLA_NORM)

    out = jax.ShapeDtypeStruct((t, D), F32)
    outb = jax.ShapeDtypeStruct((t, D), BF16)
    return pl.pallas_call(
        body, name="gates_fwd", grid=(t // TR,),
        in_specs=[seg(0), seg(1), seg(2), seg(3), seg(5), seg(6), seg(7), _rowcol(DH, LR0 // DH),
                  _full((2, 2, HW)), _full((DH, D)), _full((1, D))],
        out_specs=[_row(D)] * 6,
        out_shape=[outb] * 4 + [out] * 2,
        compiler_params=_cp("parallel"),
    )(p, p, p, p, p, p, p, p, hg_lb, wgk, bgk)


def _post_fwd(o_fw, o_bw, p, onw):
    t = o_fw.shape[0]

    def body(of_ref, ob_ref, g1_ref, g2_ref, w_ref, y_ref):
        for h in range(NH):
            sl = slice(h * DH, (h + 1) * DH)
            o = of_ref[:, sl] + ob_ref[:, sl]
            g_ref = g1_ref if h < NH // 2 else g2_ref
            gs = slice((h % (NH // 2)) * DH, (h % (NH // 2) + 1) * DH)
            n = o * _rstd(o) * w_ref[:, sl]
            y_ref[:, sl] = (n * _silu(g_ref[:, gs].astype(F32))).astype(BF16)

    return pl.pallas_call(
        body, name="post_fwd", grid=(t // TR,),
        in_specs=[_row(D), _row(D), _rowcol(HW, MAIN0 // HW + 4), _rowcol(HW, MAIN0 // HW + 8), _full((1, D))],
        out_specs=_row(D),
        out_shape=jax.ShapeDtypeStruct((t, D), BF16),
        compiler_params=_cp("parallel"),
    )(o_fw, o_bw, p, p, onw)


def _gate_window_specs(col0):
    return [_rowcol(HW, col0 // HW), _rowcol(HW, col0 // HW + 1), _rowcol(DH, (col0 + 2 * HW) // DH)]


def _gate_window(refs):
    return jnp.concatenate([r[...].astype(F32) for r in refs], axis=1)


def _branch_merge(y, w_hg, w_gla, p):
    t = y.shape[0]

    def body(y_ref, wh_ref, wg_ref, a0, a1, a2, b0, b1, b2, u1_ref, u2_ref, m_ref):
        u1 = _dot(y_ref[:, :HW], wh_ref[...], NN)
        u2 = _dot(y_ref[:, HW:], wg_ref[...], NN)
        u1_ref[...] = u1.astype(BF16)
        u2_ref[...] = u2.astype(BF16)
        m_ref[...] = (_sig(_gate_window((a0, a1, a2))) * u1 + _sig(_gate_window((b0, b1, b2))) * u2).astype(BF16)

    out = jax.ShapeDtypeStruct((t, GW), BF16)
    return pl.pallas_call(
        body, name="branch_merge", grid=(t // TR,),
        in_specs=[_row(D), _full((HW, GW)), _full((HW, GW))] + _gate_window_specs(GATE_HG0)
        + _gate_window_specs(GATE_GLA0),
        out_specs=[_row(GW)] * 3, out_shape=[out] * 3,
        compiler_params=_cp("parallel"),
    )(y, w_hg, w_gla, p, p, p, p, p, p)


def _mid_fwd(z, y1, nw_post, nw_pre, modc, modx):
    t = y1.shape[0]

    def body(zc_ref, zx_ref, y_ref, wpo_ref, wpr_ref, mc_ref, mx_ref, z1_ref, h_ref):
        is_ctx = pl.program_id(0) < NCT
        y = y_ref[...].astype(F32)
        z1 = _z_tile(zc_ref, zx_ref, is_ctx) + _mod_row(mc_ref, mx_ref, 2, is_ctx) * (y * _rstd(y) * wpo_ref[...])
        z1_ref[...] = z1
        n = z1 * _rstd(z1) * wpr_ref[...]
        h = n * (1.0 + _mod_row(mc_ref, mx_ref, 4, is_ctx)) + _mod_row(mc_ref, mx_ref, 3, is_ctx)
        h_ref[...] = h.astype(BF16)

    return pl.pallas_call(
        body, name="mid_fwd", grid=(t // TR,),
        in_specs=_z_specs() + [_row(D), _full((1, D)), _full((1, D)), _full((8, D)), _full((8, D))],
        out_specs=[_row(D), _row(D)],
        out_shape=[jax.ShapeDtypeStruct((t, D), F32), jax.ShapeDtypeStruct((t, D), BF16)],
        compiler_params=_cp("parallel"),
    )(*z, y1, nw_post, nw_pre, modc, modx)


def _final(z1, y2, target, nw, modc, modx):
    t = z1.shape[0]

    def body(z1_ref, y_ref, tg_ref, w_ref, mc_ref, mx_ref, dz_ref, dy_ref, loss_ref, sm_ref):
        i = pl.program_id(0)
        is_ctx = i < NCT

        @pl.when(i == 0)
        def _():
            loss_ref[...] = jnp.zeros_like(loss_ref)
            sm_ref[...] = jnp.zeros_like(sm_ref)

        g = _mod_row(mc_ref, mx_ref, 5, is_ctx)
        y = y_ref[...].astype(F32)
        r = _rstd(y)
        w = w_ref[...]
        yr = y * r
        n = yr * w
        e = z1_ref[...] + g * n - tg_ref[...]
        lat = jnp.where(is_ctx, 0.0, 1.0)
        loss_ref[...] += lat * _colsum(e * e)
        dz = e * (lat / D)
        dz_ref[...] = dz
        _acc_mod(sm_ref, 0, is_ctx, _colsum(dz * n))
        dn = dz * g
        _acc_row(sm_ref, 2, _colsum(dn * yr))
        dy_ref[...] = _rms_bwd(dn * w, y, r).astype(BF16)

    return pl.pallas_call(
        body, name="final", grid=(t // TR,),
        in_specs=[_row(D), _row(D), pl.BlockSpec((TR, D), lambda i: (jnp.maximum(i - NCT, 0), 0)),
                  _full((1, D)), _full((8, D)), _full((8, D))],
        out_specs=[_row(D), _row(D), _full((1, D)), _full((8, D))],
        out_shape=[jax.ShapeDtypeStruct((t, D), F32), jax.ShapeDtypeStruct((t, D), BF16),
                   jax.ShapeDtypeStruct((1, D), F32), jax.ShapeDtypeStruct((8, D), F32)],
        compiler_params=_cp("arbitrary"),
    )(z1, y2, target, nw, modc, modx)


def _mid_bwd(dh2, dz, z1, y1, nw_post, nw_pre, modc, modx):
    t = z1.shape[0]

    def body(dh_ref, dz_ref, z1_ref, y_ref, wpo_ref, wpr_ref, mc_ref, mx_ref, dzo_ref, dy_ref, sm_ref):
        i = pl.program_id(0)
        is_ctx = i < NCT

        @pl.when(i == 0)
        def _():
            sm_ref[...] = jnp.zeros_like(sm_ref)

        dh = dh_ref[...].astype(F32)
        z1 = z1_ref[...]
        r = _rstd(z1)
        zr = z1 * r
        wpr = wpr_ref[...]
        n = zr * wpr
        _acc_mod(sm_ref, 0, is_ctx, _colsum(dh))
        _acc_mod(sm_ref, 2, is_ctx, _colsum(dh * n))
        dn = dh * (1.0 + _mod_row(mc_ref, mx_ref, 4, is_ctx))
        _acc_row(sm_ref, 6, _colsum(dn * zr))
        dz1 = dz_ref[...] + _rms_bwd(dn * wpr, z1, r)
        dzo_ref[...] = dz1
        y = y_ref[...].astype(F32)
        r1 = _rstd(y)
        yr = y * r1
        wpo = wpo_ref[...]
        g = _mod_row(mc_ref, mx_ref, 2, is_ctx)
        _acc_mod(sm_ref, 4, is_ctx, _colsum(dz1 * (yr * wpo)))
        dn1 = dz1 * g
        _acc_row(sm_ref, 7, _colsum(dn1 * yr))
        dy_ref[...] = _rms_bwd(dn1 * wpo, y, r1).astype(BF16)

    return pl.pallas_call(
        body, name="mid_bwd", grid=(t // TR,),
        in_specs=[_row(D)] * 4 + [_full((1, D)), _full((1, D)), _full((8, D)), _full((8, D))],
        out_specs=[_row(D), _row(D), _full((8, D))],
        out_shape=[jax.ShapeDtypeStruct((t, D), F32), jax.ShapeDtypeStruct((t, D), BF16),
                   jax.ShapeDtypeStruct((8, D), F32)],
        compiler_params=_cp("arbitrary"),
    )(dh2, dz, z1, y1, nw_post, nw_pre, modc, modx)


def _pre_bwd(dh1, dz, z, nw, modc, modx):
    t = dh1.shape[0]

    def body(dh_ref, dz_ref, zc_ref, zx_ref, w_ref, mc_ref, mx_ref, dzo_ref, sm_ref):
        i = pl.program_id(0)
        is_ctx = i < NCT

        @pl.when(i == 0)
        def _():
            sm_ref[...] = jnp.zeros_like(sm_ref)

        dh = dh_ref[...].astype(F32)
        x = _z_tile(zc_ref, zx_ref, is_ctx)
        r = _rstd(x)
        xr = x * r
        w = w_ref[...]
        _acc_mod(sm_ref, 0, is_ctx, _colsum(dh))
        _acc_mod(sm_ref, 2, is_ctx, _colsum(dh * (xr * w)))
        dn = dh * (1.0 + _mod_row(mc_ref, mx_ref, 1, is_ctx))
        _acc_row(sm_ref, 4, _colsum(dn * xr))
        dzo_ref[...] = dz_ref[...] + _rms_bwd(dn * w, x, r)

    return pl.pallas_call(
        body, name="pre_bwd", grid=(t // TR,),
        in_specs=[_row(D)] * 2 + _z_specs() + [_full((1, D)), _full((8, D)), _full((8, D))],
        out_specs=[pl.BlockSpec((TR, D), lambda i: (jnp.maximum(i - NCT, 0), 0)), _full((8, D))],
        out_shape=[jax.ShapeDtypeStruct((t - CTX, D), F32), jax.ShapeDtypeStruct((8, D), F32)],
        compiler_params=_cp("arbitrary"),
    )(dh1, dz, *z, nw, modc, modx)


def _branch_merge_bwd(dm, p, u1, u2, w_hg, w_gla):
    t = dm.shape[0]

    def body(dm_ref, a0, a1, a2, b0, b1, b2, u1_ref, u2_ref, wh_ref, wg_ref, du1_ref, du2_ref, dg_ref, dyh_ref, dyg_ref):
        dm_ = dm_ref[...].astype(F32)
        s1 = _sig(_gate_window((a0, a1, a2)))
        s2 = _sig(_gate_window((b0, b1, b2)))
        du1 = (dm_ * s1).astype(BF16)
        du2 = (dm_ * s2).astype(BF16)
        du1_ref[...] = du1
        du2_ref[...] = du2
        dg_ref[:, :GW] = (dm_ * u1_ref[...].astype(F32) * s1 * (1.0 - s1)).astype(BF16)
        dg_ref[:, GW:] = (dm_ * u2_ref[...].astype(F32) * s2 * (1.0 - s2)).astype(BF16)
        dyh_ref[...] = _dot(du1, wh_ref[...], NT).astype(BF16)
        dyg_ref[...] = _dot(du2, wg_ref[...], NT).astype(BF16)

    return pl.pallas_call(
        body, name="branch_merge_bwd", grid=(t // TR,),
        in_specs=[_row(GW)] + _gate_window_specs(GATE_HG0) + _gate_window_specs(GATE_GLA0)
        + [_row(GW), _row(GW), _full((HW, GW)), _full((HW, GW))],
        out_specs=[_row(GW), _row(GW), _row(2 * GW), _row(HW), _row(HW)],
        out_shape=[jax.ShapeDtypeStruct((t, GW), BF16), jax.ShapeDtypeStruct((t, GW), BF16),
                   jax.ShapeDtypeStruct((t, 2 * GW), BF16), jax.ShapeDtypeStruct((t, HW), BF16),
                   jax.ShapeDtypeStruct((t, HW), BF16)],
        compiler_params=_cp("parallel"),
    )(dm, p, p, p, p, p, p, u1, u2, w_hg, w_gla)


def _post_bwd(dy_hg, dy_gla, o_fw, o_bw, p, onw):
    t = o_fw.shape[0]

    def body(d1_ref, d2_ref, of_ref, ob_ref, g1_ref, g2_ref, w_ref, do_ref, dg_ref, sm_ref):
        @pl.when(pl.program_id(0) == 0)
        def _():
            sm_ref[...] = jnp.zeros_like(sm_ref)

        for h in range(NH):
            sl = slice(h * DH, (h + 1) * DH)
            gs = slice((h % (NH // 2)) * DH, (h % (NH // 2) + 1) * DH)
            g_ref, d_ref = (g1_ref, d1_ref) if h < NH // 2 else (g2_ref, d2_ref)
            o = of_ref[:, sl] + ob_ref[:, sl]
            r = _rstd(o)
            orr = o * r
            w = w_ref[:, sl]
            gt = g_ref[:, gs].astype(F32)
            dy = d_ref[:, gs].astype(F32)
            dg_ref[:, sl] = (dy * (orr * w) * _dsilu(gt)).astype(BF16)
            dn = dy * _silu(gt)
            sm_ref[0:1, sl] += _colsum(dn * orr)
            do_ref[:, sl] = _rms_bwd(dn * w, o, r)

    return pl.pallas_call(
        body, name="post_bwd", grid=(t // TR,),
        in_specs=[_row(HW), _row(HW), _row(D), _row(D), _rowcol(HW, MAIN0 // HW + 4), _rowcol(HW, MAIN0 // HW + 8),
                  _full((1, D))],
        out_specs=[_row(D), _row(D), _full((8, D))],
        out_shape=[jax.ShapeDtypeStruct((t, D), F32), jax.ShapeDtypeStruct((t, D), BF16),
                   jax.ShapeDtypeStruct((8, D), F32)],
        compiler_params=_cp("arbitrary"),
    )(dy_hg, dy_gla, o_fw, o_bw, p, p, onw)


def _gates_bwd(p, hg_lb, wgk, bgk, dgm, dgo, dq_f, dq_b, dv_f, dv_b, dk_f, dk_b, dg_f, dg_b):
    t = p.shape[0]
    seg = lambda j: _rowcol(HW, MAIN0 // HW + j)

    def body(hq_ref, hf_ref, hb_ref, lr_ref, lb_ref, wgk_ref, bgk_ref, dgm_ref, dgo_ref,
             dqf_ref, dqb_ref, dvf_ref, dvb_ref, dkf_ref, dkb_ref, dgf_ref, dgb_ref,
             dp_ref, dlb_ref, dw_ref, db_ref):
        @pl.when(pl.program_id(0) == 0)
        def _():
            dlb_ref[...] = jnp.zeros_like(dlb_ref)
            dw_ref[...] = jnp.zeros_like(dw_ref)
            db_ref[...] = jnp.zeros_like(db_ref)

        c0 = MAIN0

        def put(j, val):
            dp_ref[:, c0 + j * HW:c0 + (j + 1) * HW] = val.astype(BF16)

        dq = dqf_ref[...].astype(F32) + dqb_ref[...].astype(F32)
        dv = dvf_ref[...].astype(F32) + dvb_ref[...].astype(F32)
        put(0, dq[:, :HW] * _dsilu(hq_ref[...].astype(F32)))
        put(1, dv[:, :HW])
        put(5, dq[:, HW:] * (DH ** -0.5))
        put(7, dv[:, HW:])
        put(6, dkf_ref[:, HW:].astype(F32) + dkb_ref[:, HW:].astype(F32))
        dp_ref[:, c0 + 4 * HW:c0 + 5 * HW] = dgo_ref[:, :HW]
        dp_ref[:, c0 + 8 * HW:c0 + 9 * HW] = dgo_ref[:, HW:]
        lr = lr_ref[...].astype(BF16)
        xg = _dot(lr, wgk_ref[...], NN) + bgk_ref[...]
        dxg = []
        for d, (raw_ref, dk_ref, dg_ref) in enumerate(((hf_ref, dkf_ref, dgf_ref), (hb_ref, dkb_ref, dgb_ref))):
            lbd = _hg_lb(lb_ref, d)
            s = _sig(raw_ref[...].astype(F32))
            f = lbd + (1.0 - lbd) * s
            df = dg_ref[:, :HW] / f - dk_ref[:, :HW].astype(F32)
            put(2 + d, df * (1.0 - lbd) * s * (1.0 - s))
            dlb_ref[d:d + 1, :] += _colsum(df * (1.0 - s)) * (lbd * (1.0 - lbd))
            dxg.append(dg_ref[:, HW:] * (1.0 / GLA_NORM) * _sig(-xg[:, d * HW:(d + 1) * HW]))
        dxg = jnp.concatenate(dxg, axis=1)
        db_ref[0:1, :] += _colsum(dxg)
        dxg_b = dxg.astype(BF16)
        dw_ref[...] += _dot(lr, dxg_b, TN)
        dlr = _dot(dxg_b, wgk_ref[...], NT)
        dp_ref[:, LR0:LR0 + DH] = (dlr + dgm_ref[:, :DH].astype(F32)).astype(BF16)
        dp_ref[:, LR0 + DH:GATE_GLA0] = dgm_ref[:, DH:D]
        dp_ref[:, GATE_GLA0:GATE_GLA0 + DH] = dgm_ref[:, D:GW] + dgm_ref[:, GW:GW + DH]
        dp_ref[:, GATE_GLA0 + DH:GATE_GLA0 + GW] = dgm_ref[:, GW + DH:]
        dp_ref[:, GATE_GLA0 + GW:] = jnp.zeros((TR, W_IN_COLS - GATE_GLA0 - GW), BF16)

    return pl.pallas_call(
        body, name="gates_bwd", grid=(t // TR,),
        in_specs=[seg(0), seg(2), seg(3), _rowcol(DH, LR0 // DH), _full((2, 2, HW)), _full((DH, D)), _full((1, D)),
                  _row(2 * GW), _row(D)] + [_row(D)] * 8,
        out_specs=[_row(W_IN_COLS), _full((8, HW)), _full((DH, D)), _full((8, D))],
        out_shape=[jax.ShapeDtypeStruct((t, W_IN_COLS), BF16), jax.ShapeDtypeStruct((8, HW), F32),
                   jax.ShapeDtypeStruct((DH, D), F32), jax.ShapeDtypeStruct((8, D), F32)],
        compiler_params=_cp("arbitrary"),
    )(p, p, p, p, hg_lb, wgk, bgk, dgm, dgo, dq_f, dq_b, dv_f, dv_b, dk_f, dk_b, dg_f, dg_b)


def _scan_consts(rev):
    r = lax.broadcasted_iota(jnp.int32, (CHUNK, CHUNK), 0)
    u = lax.broadcasted_iota(jnp.int32, (CHUNK, CHUNK), 1)
    rp = lax.broadcasted_iota(jnp.int32, (CHUNK, 1), 0)
    if rev:
        r, u, rp = CHUNK - 1 - r, CHUNK - 1 - u, CHUNK - 1 - rp
    tri = jnp.where(u <= r, 1.0, 0.0).astype(F32)
    tri_t = jnp.where(r <= u, 1.0, 0.0).astype(F32)
    lv = []
    for b in LEVELS:
        sh = b.bit_length() - 1
        pair = ((r >> sh) == (u >> sh) + 1) & (((u >> sh) & 1) == 0)
        pair_t = ((u >> sh) == (r >> sh) + 1) & (((r >> sh) & 1) == 0)
        tside = ((rp >> sh) & 1) == 1
        lv.append((pair, pair_t, tside, jnp.where(tside, 1.0, -1.0).astype(F32)))
    bd = LEVELS[-1].bit_length() - 1
    diag = ((r >> bd) == (u >> bd)) & (u <= r)
    diag_t = ((r >> bd) == (u >> bd)) & (r <= u)
    return tri, tri_t, lv, diag, diag_t


def _row_of(pos, rev):
    return CHUNK - 1 - pos if rev else pos


def _chunk_terms(cum, b_scr, consts, rev):
    _, _, lv, _, _ = consts
    terms = []
    for b, (_, _, _, sgn) in zip(LEVELS, lv):
        pieces = []
        for j in range(CHUNK // (2 * b)):
            row = _row_of(2 * b * j + b - 1, rev)
            pieces.append(jnp.broadcast_to(b_scr[row:row + 1, :], (2 * b, DH)))
        if rev:
            pieces = pieces[::-1]
        bnd = pieces[0] if len(pieces) == 1 else jnp.concatenate(pieces, axis=0)
        terms.append(jnp.exp((cum - bnd) * sgn))
    b = LEVELS[-1]
    pieces = []
    for j in range(CHUNK // b):
        if j == 0:
            pieces.append(jnp.zeros((b, DH), F32))
        else:
            row = _row_of(b * j - 1, rev)
            pieces.append(jnp.broadcast_to(b_scr[row:row + 1, :], (b, DH)))
    if rev:
        pieces = pieces[::-1]
    start = jnp.concatenate(pieces, axis=0)
    wq = jnp.exp(jnp.minimum(cum - start, 0.0))
    wk = jnp.exp(jnp.minimum(start - cum, EXP_CLAMP))
    terms.append((wq, wk))
    return terms


def _run_staged(units):
    live = list(units)
    while live:
        nxt = []
        for u in live:
            try:
                next(u)
                nxt.append(u)
            except StopIteration:
                pass
        live = nxt


SCAN_TB = 256
SCAN_CB = SCAN_TB // CHUNK


def _block_order(i, ntb, rev):
    nctx = CTX // SCAN_TB
    if not rev:
        return i
    return jnp.where(i < nctx, nctx - 1 - i, ntb - 1 - (i - nctx))


def _chunk_in_block(j, rev):
    return SCAN_CB - 1 - j if rev else j


def _scan_fwd(q, k, v, g, rev):
    t = q.shape[0]
    nc = t // CHUNK
    hpb = SCAN_HEADS_FWD

    def body(q_ref, k_ref, v_ref, g_ref, o_ref, st_ref, s_scr, b_scr):
        consts = _scan_consts(rev)
        _, _, lv, diag, _ = consts
        masks = [lvl[0] for lvl in lv] + [diag]

        @pl.when(pl.program_id(1) == 0)
        def _():
            s_scr[...] = jnp.zeros_like(s_scr)

        tri = consts[0]
        state = {hh: s_scr[hh] for hh in range(hpb)}

        def unit(hh, j):
            sl = slice(hh * DH, (hh + 1) * DH)
            c = _chunk_in_block(j, rev)
            rows = slice(c * CHUNK, (c + 1) * CHUNK)
            b_ref = b_scr.at[hh * SCAN_CB + j]
            qc, kc, vc, gc = q_ref[rows, sl], k_ref[rows, sl], v_ref[rows, sl], g_ref[rows, sl]
            cum = _split_dot(tri, gc)
            b_ref[...] = cum
            yield
            terms = _chunk_terms(cum, b_ref, consts, rev)
            qf, kf = qc.astype(F32), kc.astype(F32)
            xs = [(jnp.where(tside, qf, kf) * w).astype(BF16) for w, (_, _, tside, _) in zip(terms[:-1], lv)]
            qd, kd = (qf * terms[-1][0]).astype(BF16), (kf * terms[-1][1]).astype(BF16)
            tot = _colsum(gc)
            qe = (qf * jnp.exp(cum)).astype(BF16)
            ke = (kf * jnp.exp(tot - cum)).astype(BF16)
            vb = vc.astype(BF16)
            yield
            scs = [_dot(x, x, NT) for x in xs] + [_dot(qd, kd, NT)]
            kv = _dot(vb, ke, TN)
            yield
            a = jnp.zeros((CHUNK, CHUNK), F32)
            for sc, m in zip(scs, masks):
                a = a + jnp.where(m, sc, 0.0)
            o_intra = _dot(a.astype(BF16), vb, NN)
            yield
            st = state[hh]
            st_ref[hh, c] = st
            o_ref[rows, sl] = o_intra + _dot(qe, st.astype(BF16), NT)
            state[hh] = st * jnp.exp(tot) + kv
            yield

        _run_staged([unit(hh, j) for hh in range(hpb) for j in range(SCAN_CB)])
        for hh in range(hpb):
            s_scr[hh] = state[hh]

    ntb = t // SCAN_TB
    col = pl.BlockSpec((SCAN_TB, hpb * DH), lambda h, i: (_block_order(i, ntb, rev), h))
    return pl.pallas_call(
        body, name="scan_fwd_" + ("bw" if rev else "fw"), grid=(NH // hpb, ntb),
        in_specs=[col] * 4,
        out_specs=[col, pl.BlockSpec((hpb, SCAN_CB, DH, DH), lambda h, i: (h, _block_order(i, ntb, rev), 0, 0))],
        out_shape=[jax.ShapeDtypeStruct((t, D), F32), jax.ShapeDtypeStruct((NH, nc, DH, DH), F32)],
        scratch_shapes=[pltpu.VMEM((hpb, DH, DH), F32), pltpu.VMEM((hpb * SCAN_CB, CHUNK, DH), F32)],
        compiler_params=_cp("parallel", "arbitrary"),
    )(q, k, v, g)


def _scan_bwd(q, k, v, g, do, states, rev):
    t = q.shape[0]
    nc = t // CHUNK
    hpb = SCAN_HEADS_BWD

    def body(q_ref, k_ref, v_ref, g_ref, do_ref, st_ref, dq_ref, dk_ref, dv_ref, dg_ref, ds_scr, b_scr):
        consts = _scan_consts(rev)
        _, tri_t, lv, diag, diag_t = consts
        masks = [(lvl[0], lvl[1]) for lvl in lv] + [(diag, diag_t)]
        @pl.when(pl.program_id(1) == 0)
        def _():
            ds_scr[...] = jnp.zeros_like(ds_scr)

        tri = consts[0]
        dstate = {hh: ds_scr[hh] for hh in range(hpb)}

        def unit(hh, jj):
            sl = slice(hh * DH, (hh + 1) * DH)
            c = _chunk_in_block(SCAN_CB - 1 - jj, rev)
            rows = slice(c * CHUNK, (c + 1) * CHUNK)
            b_ref = b_scr.at[hh * SCAN_CB + jj]
            qc, kc, vc, gc = q_ref[rows, sl], k_ref[rows, sl], v_ref[rows, sl], g_ref[rows, sl]
            dob = do_ref[rows, sl].astype(BF16)
            vb = vc.astype(BF16)
            cum = _split_dot(tri, gc)
            b_ref[...] = cum
            da = _dot(dob, vb, NT)
            da_t = _dot(vb, dob, NT)
            yield
            terms = _chunk_terms(cum, b_ref, consts, rev)
            qf, kf = qc.astype(F32), kc.astype(F32)
            xs = [(jnp.where(tside, qf, kf) * w).astype(BF16) for w, (_, _, tside, _) in zip(terms[:-1], lv)]
            wqd, wkd = terms[-1]
            qdb, kdb = (qf * wqd).astype(BF16), (kf * wkd).astype(BF16)
            tot = _colsum(gc)
            e_tot = jnp.exp(tot)
            e_b = jnp.exp(cum)
            e_t = jnp.exp(tot - cum)
            qeb = (qf * e_b).astype(BF16)
            keb = (kf * e_t).astype(BF16)
            dsym = [(jnp.where(m, da, 0.0) + jnp.where(m_t, da_t, 0.0)).astype(BF16) for m, m_t in masks[:-1]]
            dad = (jnp.where(diag, da, 0.0).astype(BF16), jnp.where(diag_t, da_t, 0.0).astype(BF16))
            yield
            sym = [_dot(x, x, NT) for x in xs]
            dxs = [_dot(d, x, NN) for d, x in zip(dsym, xs)]
            at_d = _dot(kdb, qdb, NT)
            dqt_d = _dot(dad[0], kdb, NN)
            dkt_d = _dot(dad[1], qdb, NN)
            qd = _dot(dob, qeb, TN)
            yield
            a_t = jnp.where(diag_t, at_d, 0.0)
            dq = dqt_d * wqd
            dk = dkt_d * wkd
            db = dqt_d * qdb.astype(F32) - dkt_d * kdb.astype(F32)
            for s, dx, x, w, (_, m_t, tside, sgn) in zip(sym, dxs, xs, terms[:-1], lv):
                a_t = a_t + jnp.where(m_t, s, 0.0)
                dxw = dx * w
                dq = dq + jnp.where(tside, dxw, 0.0)
                dk = dk + jnp.where(tside, 0.0, dxw)
                db = db + (dx * x.astype(F32)) * sgn
            dv_intra = _dot(a_t.astype(BF16), dob, NN)
            st = st_ref[hh, c]
            stb = st.astype(BF16)
            dqe = _dot(dob, stb, NN)
            yield
            dst = dstate[hh]
            dstb = dst.astype(BF16)
            dstate[hh] = dst * e_tot + qd
            dv_ref[rows, sl] = (dv_intra + _dot(keb, dstb, NT)).astype(BF16)
            dke = _dot(vb, dstb, NN)
            yield
            qe = qeb.astype(F32)
            ke = keb.astype(F32)
            dq_ref[rows, sl] = (dq + dqe * e_b).astype(BF16)
            dk_ref[rows, sl] = (dk + dke * e_t).astype(BF16)
            db = db + dqe * qe - dke * ke
            dtot = _colsum(dstb.astype(F32) * stb.astype(F32)) * e_tot + _colsum(dke * ke)
            dg_ref[rows, sl] = _split_dot(tri_t, db) + dtot
            yield

        _run_staged([unit(hh, jj) for hh in range(hpb) for jj in range(SCAN_CB)])
        for hh in range(hpb):
            ds_scr[hh] = dstate[hh]

    ntb = t // SCAN_TB
    blk = lambda i: _block_order(ntb - 1 - i, ntb, rev)
    col = pl.BlockSpec((SCAN_TB, hpb * DH), lambda h, i: (blk(i), h))
    out = jax.ShapeDtypeStruct((t, D), F32)
    outb = jax.ShapeDtypeStruct((t, D), BF16)
    return pl.pallas_call(
        body, name="scan_bwd_" + ("bw" if rev else "fw"), grid=(NH // hpb, ntb),
        in_specs=[col] * 5 + [pl.BlockSpec((hpb, SCAN_CB, DH, DH), lambda h, i: (h, blk(i), 0, 0))],
        out_specs=[col] * 4,
        out_shape=[outb] * 3 + [out],
        scratch_shapes=[pltpu.VMEM((hpb, DH, DH), F32), pltpu.VMEM((hpb * SCAN_CB, CHUNK, DH), F32)],
        compiler_params=_cp("parallel", "arbitrary"),
    )(q, k, v, g, do, states)


W_IN_GRAD_CHUNKS = (("a", (0, 512)), ("b", (0, 256)), ("b", (256, 512)))
W_IN_REF = 6688
W_IN_PAD = 896
W_IN_PIECE = 256
W_IN_STAGES = (3, 4)


def _assemble_w_in(g, rows, prev, name):
    n, r, wp = g.shape
    tr = W_IN_PIECE
    tiles = wp // DH
    first = rows[0] // tr

    def body(g_ref, *refs):
        o_ref = refs[-1]
        lane = lax.broadcasted_iota(jnp.int32, (tr, DH), 1)
        for t in range(W_IN_COLS // DH):
            acc = None
            for j in range(n):
                c = DH * t - W_IN_SHARD * j
                if c <= -DH or c >= W_IN_SHARD:
                    continue
                k, s = divmod(c, DH)
                lo = g_ref[j, :, k * DH:(k + 1) * DH] if 0 <= k < tiles else None
                hi = g_ref[j, :, (k + 1) * DH:(k + 2) * DH] if s and 0 <= k + 1 < tiles else None
                if s:
                    zero = jnp.zeros((tr, DH), g.dtype)
                    lo = zero if lo is None else pltpu.roll(lo, DH - s, 1)
                    hi = zero if hi is None else pltpu.roll(hi, DH - s, 1)
                    part = jnp.where(lane < DH - s, lo, hi)
                else:
                    part = lo
                acc = part if acc is None else acc + part
            o_ref[:, t * DH:(t + 1) * DH] = jnp.zeros((tr, DH), g.dtype) if acc is None else acc

    held = [] if prev is None else [prev]
    return pl.pallas_call(
        body, name=name, grid=((rows[1] - rows[0]) // tr,),
        in_specs=[pl.BlockSpec((n, tr, wp), lambda i: (0, first + i, 0))] + [pl.BlockSpec(memory_space=pl.ANY)] * len(held),
        out_specs=pl.BlockSpec((tr, W_IN_COLS), lambda i: (first + i, 0)),
        out_shape=jax.ShapeDtypeStruct((r, W_IN_COLS), g.dtype),
        input_output_aliases={1: 0} if held else {},
        compiler_params=_cp("parallel"),
    )(g, *held)


def _gate_cols(w):
    return jnp.pad(w, ((0, 0), (GOFF, GW - GOFF - D)))


def _gate_rows(w):
    return jnp.pad(w, ((GOFF, GW - GOFF - D), (0, 0)))


def _layout_wgk(w):
    r = w.shape[1]
    top = jnp.concatenate([w[0], jnp.zeros_like(w[0])], axis=1)
    bot = jnp.concatenate([jnp.zeros_like(w[1]), w[1]], axis=1)
    return jnp.concatenate([top, bot, jnp.zeros((DH - 2 * r, D), w.dtype)], axis=0)


def _unlayout_wgk(d, r=16):
    return jnp.stack([d[:r, :HW], d[r:2 * r, HW:]])


def _local_step(z, target, modc, modx, norms, onw, hg_lb, wgk, bgk, get_w_in, get_mix, get_ffn, send):
    n_pre1, n_post1, n_pre2, n_post2 = norms
    t = z[0].shape[0] + z[1].shape[0]
    tm = 1152 if t % 1152 == 0 else 256
    h1 = _prenorm(z, n_pre1, modc, modx, 0, 1, "prenorm1")
    w_in = get_w_in(h1)
    p = _matmul(h1, w_in, NN, BF16, "mm_in", t, 1024, D)
    q, v, k_f, k_b, g_f, g_b = _gates_fwd(p, hg_lb, wgk, bgk)
    o_f, st_f = _scan_fwd(q, k_f, v, g_f, False)
    o_b, st_b = _scan_fwd(q, k_b, v, g_b, True)
    y = _post_fwd(o_f, o_b, p, onw)
    w_br_hg, w_br_gla, w_out = get_mix(y)
    u1, u2, merged = _branch_merge(y, w_br_hg, w_br_gla, p)
    y1 = _matmul(merged, w_out, NN, BF16, "mm_out", tm, 512, GW)
    z1, h2 = _mid_fwd(z, y1, n_post1, n_pre2, modc, modx)
    w_gu_t, get_down = get_ffn(h2)
    u, v_ff, act = _mm_gu_act(h2, w_gu_t[0], w_gu_t[1], "mm_gu", tm)
    w_down = get_down(act)
    y2 =_matmul(act, w_down, NN, BF16, "mm_down", t, 512, D_FF)
    dz, dy2, loss_vec, sm_final = _final(z1, y2, target, n_post2, modc, modx)
    du, dv_ff = _mm_down_dx_act(dy2, w_down, u, v_ff, "mm_down_dx", tm)
    d_w_down = _matmul(act, dy2, TN, BF16, "mm_down_dw", D_FF // 2, 1024, t)
    dh2 = _matmul((du, dv_ff), w_gu_t, NN, BF16, "mm_gu_dx", tm, 512, D_FF)
    d_w_gate_t = _matmul(du, h2, TN, BF16, "mm_gate_dw", D_FF // 2, 1024, t)
    d_w_up_t = _matmul(dv_ff, h2, TN, BF16, "mm_up_dw", D_FF // 2, 1024, t)
    dh2 = send(("w_down", "w_gate_t", "w_up_t"), (d_w_down, d_w_gate_t, d_w_up_t), dh2)
    dz, dy1, sm_mid = _mid_bwd(dh2, dz, z1, y1, n_post1, n_pre2, modc, modx)
    dmerged = _matmul(dy1, w_out, NT, BF16, "mm_out_dx", tm, GW, D)
    d_w_out = _matmul(merged, dy1, TN, BF16, "mm_out_dw", GW, 512, t)
    du1, du2, dgm, dy_hg, dy_gla = _branch_merge_bwd(dmerged, p, u1, u2, w_br_hg, w_br_gla)
    d_w_br_hg = _matmul(y, du1, TN, BF16, "mm_br_hg_dw", HW, GW, t, a_off=0, m_out=HW)
    d_w_br_gla = _matmul(y, du2, TN, BF16, "mm_br_gla_dw", HW, GW, t, a_off=1, m_out=HW)
    dy_hg = send(("w_out", "w_br_hg", "w_br_gla"), (d_w_out, d_w_br_hg, d_w_br_gla), dy_hg)
    do, dgo, sm_post = _post_bwd(dy_hg, dy_gla, o_f, o_b, p, onw)
    dq_f, dk_f, dv_f, dg_f = _scan_bwd(q, k_f, v, g_f, do, st_f, False)
    dq_b, dk_b, dv_b, dg_b = _scan_bwd(q, k_b, v, g_b, do, st_b, True)
    dp, d_lb, d_wgk, d_bgk = _gates_bwd(p, hg_lb, wgk, bgk, dgm, dgo, dq_f, dq_b, dv_f, dv_b, dk_f, dk_b, dg_f, dg_b)
    d_w_in_a = _matmul(h1, dp, TN, BF16, "mm_in_dw_a", 512, 1024, t, a_off=0, m_out=D // 2)
    dp = send(("w_in_a",), (d_w_in_a,), dp)
    d_w_in_b = _matmul(h1, dp, TN, BF16, "mm_in_dw_b", 512, 1024, t, a_off=1, m_out=D // 2)
    dp = send(("w_in_b",), (d_w_in_b,), dp)
    dh1 = _matmul(dp, w_in, NT, BF16, "mm_in_dx", tm, 512, W_IN_COLS // 2)
    grad_x, sm_pre = _pre_bwd(dh1, dz, z, n_pre1, modc, modx)
    return dict(loss_vec=loss_vec, grad_x=grad_x, sm_final=sm_final, sm_mid=sm_mid, sm_post=sm_post, sm_pre=sm_pre,
                d_lb=d_lb, d_wgk=d_wgk, d_bgk=d_bgk)


MESH = pl.DeviceIdType.MESH
ANY = pl.BlockSpec(memory_space=pl.ANY)
N_REL = N_DEV - 1


def _place():
    return lax.axis_index("x"), lax.axis_index("y"), lax.axis_index("c")


def _slot(p):
    return 4 * p[0] + 2 * p[1] + p[2]


HBM = pl.BlockSpec(memory_space=pltpu.HBM)
SEM = pl.BlockSpec(memory_space=pltpu.SEMAPHORE)
EFFECT = pltpu.SideEffectType.DATAFLOW_SIDE_EFFECTING


def _peer_of(x, y, c, k):
    flip = lambda v, bit: 1 - v if bit else v
    return flip(x, k & 4), flip(y, k & 2), flip(c, k & 1)


def _view_whole(src, slot):
    return src


def _view_near(src, slot):
    return src


_view_near.peers = (1, 2, 4)


def _view_near_rows(rows):
    def view(src, slot):
        return src.at[pl.ds(rows[0], rows[1] - rows[0])]
    view.peers = _view_near.peers
    view.land = lambda land, slot: land.at[slot, pl.ds(rows[0], rows[1] - rows[0])]
    return view


def _view_block(src, slot):
    return src.at[slot]


def _view_cols(src, slot):
    return src.at[:, pl.ds(pl.multiple_of(slot * (D // N_DEV), D // N_DEV), D // N_DEV)]


W_IN_SHARD = W_IN_REF // N_DEV


def _view_window(rows):
    def view(src, slot):
        col0 = pl.multiple_of((W_IN_SHARD * slot // DH) * DH, DH)
        return src.at[pl.ds(rows[0], rows[1] - rows[0]), pl.ds(col0, D)]
    return view


def _split_copies(view, srcs, lands, send_sems, recv_sems, local_sems):
    x, y, c = _place()
    me = _slot((x, y, c))
    into = getattr(view, "land", lambda land, slot: land.at[slot])
    local, sends, waits = [], [], []
    for a, (src, land) in enumerate(zip(srcs, lands)):
        local.append(pltpu.make_async_copy(view(src, me), into(land, me), local_sems.at[a]))
        for k in getattr(view, "peers", range(1, N_DEV)):
            peer = _peer_of(x, y, c, k)
            mine = view(src, _slot(peer))
            sems = dict(send_sem=send_sems.at[N_REL * a + k - 1], recv_sem=recv_sems.at[N_REL * a + k - 1],
                        device_id=peer, device_id_type=MESH)
            sends.append(pltpu.make_async_remote_copy(src_ref=mine, dst_ref=into(land, me), **sems))
            waits.append(pltpu.make_async_remote_copy(src_ref=mine, dst_ref=into(land, _slot(peer)), **sems))
    return local, sends, waits


def _split_start(groups, name, after):
    built = []
    for view, srcs, lands in groups:
        lands = [lax.empty(l, s.dtype) if isinstance(l, tuple) else l for l, s in zip(lands, srcs)]
        built.append((view, list(srcs), lands))
    bufs = [b for _, srcs, lands in built for b in srcs + lands]
    nb, ng = len(bufs), len(built)

    def body(*refs):
        buf_refs, sem_refs, token = refs[:nb], refs[nb + 1:nb + 1 + 3 * ng], refs[-1]
        pos = 0
        for i, (view, srcs, _) in enumerate(built):
            n = len(srcs)
            local, sends, _ = _split_copies(view, buf_refs[pos:pos + n], buf_refs[pos + n:pos + 2 * n],
                                            *sem_refs[3 * i:3 * i + 3])
            pos += 2 * n
            for cp in local + sends:
                cp.start()
        token[...] = jnp.zeros_like(token)

    sems = []
    for _, srcs, _ in built:
        n = len(srcs)
        sems += [pltpu.SemaphoreType.DMA((N_REL * n,)), pltpu.SemaphoreType.DMA((N_REL * n,)),
                 pltpu.SemaphoreType.DMA((n,))]
    hbm = lambda a: pltpu.with_memory_space_constraint(a, pltpu.HBM)
    out = pl.pallas_call(
        body, name=name,
        out_shape=(*sems, *[pltpu.HBM(b.shape, b.dtype) for b in bufs], jax.ShapeDtypeStruct((8, DH), F32)),
        in_specs=[HBM] * nb + [ANY],
        out_specs=(*([SEM] * (3 * ng)), *([HBM] * nb), pl.BlockSpec(memory_space=pltpu.VMEM)),
        input_output_aliases={i: 3 * ng + i for i in range(nb)},
        compiler_params=pltpu.CompilerParams(has_side_effects=EFFECT),
    )(*[hbm(b) for b in bufs], after)
    handles, pos = [], 3 * ng
    for i, (view, srcs, _) in enumerate(built):
        n = len(srcs)
        handles.append(dict(view=view, n=n, sems=out[3 * i:3 * i + 3], srcs=list(out[pos:pos + n]),
                            lands=list(out[pos + n:pos + 2 * n])))
        pos += 2 * n
    return handles, out[-1]


def _split_wait(handle, name, after, srcs=None, lands=None):
    view, n, sems = handle["view"], handle["n"], handle["sems"]
    srcs = handle["srcs"] if srcs is None else srcs
    lands = handle["lands"] if lands is None else lands
    afters = list(after) if isinstance(after, (list, tuple)) else [after]

    def body(*refs):
        src_refs, land_refs = refs[:n], refs[n:2 * n]
        send_sems, recv_sems, local_sems = refs[2 * n:2 * n + 3]
        local, _, waits = _split_copies(view, src_refs, land_refs, send_sems, recv_sems, local_sems)
        for cp in waits:
            cp.wait_send()
            cp.wait_recv()
        for cp in local:
            cp.wait()

    out = pl.pallas_call(
        body, name=name,
        out_shape=(*[pltpu.HBM(s.shape, s.dtype) for s in srcs], *[pltpu.HBM(l.shape, l.dtype) for l in lands]),
        in_specs=[HBM] * (2 * n) + [SEM, SEM, SEM] + [ANY] * len(afters),
        out_specs=tuple([HBM] * (2 * n)),
        input_output_aliases={i: i for i in range(2 * n)},
        compiler_params=pltpu.CompilerParams(has_side_effects=EFFECT),
    )(*srcs, *lands, *sems, *afters)
    handle["srcs"] = list(out[:n])
    return list(out[n:])


def _tie(x, token, name):
    def body(x_ref, t_ref, o_ref):
        pass

    return pl.pallas_call(
        body, name=name, out_shape=jax.ShapeDtypeStruct(x.shape, x.dtype),
        in_specs=[ANY, ANY], out_specs=ANY, input_output_aliases={0: 0},
    )(x, token)


def _forward_diagonal(land, name, rows):
    mid = (rows[0] + rows[1]) // 2
    half_a, half_b = pl.ds(rows[0], mid - rows[0]), pl.ds(mid, rows[1] - mid)

    def body(land_ref, out_ref, send_sems, recv_sems):
        x, y, c = _place()
        diag = _slot((1 - x, 1 - y, c))

        def copy(blk, part, j, to):
            return pltpu.make_async_remote_copy(src_ref=land_ref.at[blk, part], dst_ref=out_ref.at[blk, part],
                                                send_sem=send_sems.at[j], recv_sem=recv_sems.at[j],
                                                device_id=to, device_id_type=MESH)

        sends = [copy(_slot((1 - x, y, c)), half_a, 0, (x, 1 - y, c)),
                 copy(_slot((x, 1 - y, c)), half_b, 1, (1 - x, y, c))]
        for cp in sends:
            cp.start()
        copy(diag, half_a, 0, (x, 1 - y, c)).wait_recv()
        copy(diag, half_b, 1, (1 - x, y, c)).wait_recv()
        for cp in sends:
            cp.wait_send()

    return pl.pallas_call(
        body, name=name, in_specs=[ANY], out_specs=ANY, input_output_aliases={0: 0},
        out_shape=jax.ShapeDtypeStruct(land.shape, land.dtype),
        scratch_shapes=[pltpu.SemaphoreType.DMA((2,)), pltpu.SemaphoreType.DMA((2,))],
    )(land)


def _forward_to_sibling(land, name, rows):
    def body(land_ref, out_ref, send_sems, recv_sems):
        x, y, c = _place()
        sibling = (x, y, 1 - c)
        chips = [(1 - x, y), (x, 1 - y), (1 - x, 1 - y)]
        piece = pl.ds(rows[0], rows[1] - rows[0])

        def copy(j, core):
            blk = _slot((*chips[j], core))
            return pltpu.make_async_remote_copy(src_ref=land_ref.at[blk, piece], dst_ref=out_ref.at[blk, piece],
                                                send_sem=send_sems.at[j], recv_sem=recv_sems.at[j],
                                                device_id=sibling, device_id_type=MESH)

        sends = [copy(j, c) for j in range(3)]
        for cp in sends:
            cp.start()
        for j in range(3):
            copy(j, 1 - c).wait_recv()
        for cp in sends:
            cp.wait_send()

    return pl.pallas_call(
        body, name=name, in_specs=[ANY], out_specs=ANY, input_output_aliases={0: 0},
        out_shape=jax.ShapeDtypeStruct(land.shape, land.dtype),
        scratch_shapes=[pltpu.SemaphoreType.DMA((3,)), pltpu.SemaphoreType.DMA((3,))],
    )(land)


def _mod_fwd(a, w, b):
    def body(a_ref, w_ref, b_ref, o_ref):
        o_ref[...] = _dot(_silu(a_ref[...]), w_ref[...], NN, precision=HI) + b_ref[...]

    return pl.pallas_call(
        body, name="mod_fwd", out_shape=jax.ShapeDtypeStruct((a.shape[0], w.shape[1]), F32),
        compiler_params=pltpu.CompilerParams(vmem_limit_bytes=VMEM_LIMIT),
    )(a, w, b)


def _mod_bwd(a, d, w):
    def body(a_ref, d_ref, w_ref, dw_ref, dc_ref):
        av = a_ref[...]
        dv = d_ref[...]
        dw_ref[...] = _dot(_silu(av), dv, TN, precision=HI)
        da = _dot(dv[0:8, :], w_ref[...], NT, precision=HI) * _dsilu(av[0:8, :])
        row = lax.broadcasted_iota(jnp.int32, da.shape, 0)
        dc_ref[...] = jnp.where(row == 0, da, 0.0)

    return pl.pallas_call(
        body, name="mod_bwd",
        out_shape=[jax.ShapeDtypeStruct(w.shape, F32), jax.ShapeDtypeStruct((8, w.shape[0]), F32)],
        compiler_params=pltpu.CompilerParams(vmem_limit_bytes=VMEM_LIMIT),
    )(a, d, w)


def _sum_devices(g):
    def body(g_ref, o_ref):
        acc = g_ref[0]
        for i in range(1, g.shape[0]):
            acc = acc + g_ref[i]
        o_ref[...] = acc

    return pl.pallas_call(body, name="sum_devices_%d" % g.shape[1],
                          out_shape=jax.ShapeDtypeStruct(g.shape[1:], F32))(g)


def _sum_windows(g, name):
    n, r, c = g.shape
    tr = 128

    def body(g_ref, o_ref):
        x, y, cc = _place()
        lane0 = (W_IN_SHARD * _slot((x, y, cc))) % DH
        acc = g_ref[0].astype(F32)
        for i in range(1, n):
            acc = acc + g_ref[i].astype(F32)
        o_ref[...] = pltpu.roll(acc, (c - lane0) % c, 1).T

    return pl.pallas_call(
        body, name=name, grid=(r // tr,),
        in_specs=[pl.BlockSpec((n, tr, c), lambda i: (0, i, 0))],
        out_specs=pl.BlockSpec((c, tr), lambda i: (0, i)),
        out_shape=jax.ShapeDtypeStruct((c, r), F32),
        compiler_params=_cp("parallel"),
    )(g)


def _adam_rows(r, c, n):
    budget = 10 * 1024 * 1024
    best = None
    for tr in range(16, r + 1, 16):
        if r % tr == 0 and tr * c * (2 * n + 28) <= budget:
            best = tr
    return best if best is not None else r


def _adamw(g, w, m, v, name):
    n, r, c = g.shape
    tr = _adam_rows(r, c, n)
    bc1 = 1.0 - ADAM_B1 ** ADAM_STEP
    bc2 = 1.0 - ADAM_B2 ** ADAM_STEP

    def body(g_ref, w_ref, m_ref, v_ref, go_ref, d_ref, mo_ref, vo_ref):
        grad = g_ref[0].astype(F32)
        for i in range(1, n):
            grad = grad + g_ref[i].astype(F32)
        go_ref[...] = grad
        m_new = ADAM_B1 * m_ref[...] + (1.0 - ADAM_B1) * grad
        v_new = ADAM_B2 * v_ref[...] + (1.0 - ADAM_B2) * (grad * grad)
        mo_ref[...] = m_new
        vo_ref[...] = v_new
        d_ref[...] = -ADAM_LR * ((m_new / bc1) / (jnp.sqrt(v_new / bc2) + ADAM_EPS) + ADAM_WD * w_ref[...])

    blk = pl.BlockSpec((tr, c), lambda i: (i, 0))
    out = jax.ShapeDtypeStruct((r, c), F32)
    return pl.pallas_call(
        body, name=name, grid=(r // tr,),
        in_specs=[pl.BlockSpec((n, tr, c), lambda i: (0, i, 0)), blk, blk, blk],
        out_specs=[blk] * 4, out_shape=[out] * 4,
        compiler_params=_cp("parallel"),
    )(g, w, m, v)


ADAM_ROWS3 = 168


def _adam_math(grad, w, m, v):
    bc1 = 1.0 - ADAM_B1 ** ADAM_STEP
    bc2 = 1.0 - ADAM_B2 ** ADAM_STEP
    m_new = ADAM_B1 * m + (1.0 - ADAM_B1) * grad
    v_new = ADAM_B2 * v + (1.0 - ADAM_B2) * (grad * grad)
    delta = -ADAM_LR * ((m_new / bc1) / (jnp.sqrt(v_new / bc2) + ADAM_EPS) + ADAM_WD * w)
    return delta, m_new, v_new


def _adamw_rows3(g, w3, m3, v3, name, cols, prev):
    r, _, _ = w3.shape
    c = cols[1] - cols[0]
    n = min(-(-r // 16) * 8, ADAM_ROWS3 * D // c // 8 * 8)
    starts = list(range(0, r - n, n)) + [r - n]
    held = [] if prev is None else list(prev)

    def body(g_hbm, w_hbm, m_hbm, v_hbm, *refs):
        go_hbm, d_hbm, mo_hbm, vo_hbm, gbuf, ibuf, obuf, in_sems, out_sems = refs[len(held):]
        part = lambda h, r0: h.at[pl.ds(r0, n), 0, pl.ds(cols[0], c)]

        def fetch(p):
            r0, slot = starts[p], p % 2
            g0 = (r0 // 8) * 8
            cps = [pltpu.make_async_copy(g_hbm.at[pl.ds(g0, n + 8)], gbuf.at[slot], in_sems.at[slot, 0])]
            cps += [pltpu.make_async_copy(part(h, r0), ibuf.at[slot, k], in_sems.at[slot, 1 + k])
                    for k, h in enumerate((w_hbm, m_hbm, v_hbm))]
            for cp in cps:
                cp.start()
            return cps

        pending, outs = fetch(0), []
        for p, r0 in enumerate(starts):
            slot = p % 2
            nxt = fetch(p + 1) if p + 1 < len(starts) else []
            for cp in pending:
                cp.wait()
            grad = gbuf[slot, pl.ds(r0 - (r0 // 8) * 8, n), :]
            delta, m_new, v_new = _adam_math(grad, ibuf[slot, 0], ibuf[slot, 1], ibuf[slot, 2])
            for cp in outs:
                cp.wait()
            for k, val in enumerate((grad, delta, m_new, v_new)):
                obuf[slot, k] = val
            outs = [pltpu.make_async_copy(obuf.at[slot, k], part(h, r0), out_sems.at[slot, k])
                    for k, h in enumerate((go_hbm, d_hbm, mo_hbm, vo_hbm))]
            for cp in outs:
                cp.start()
            pending = nxt
        for cp in outs:
            cp.wait()

    out = jax.ShapeDtypeStruct(w3.shape, F32)
    return pl.pallas_call(
        body, name=name, in_specs=[ANY] * (4 + len(held)), out_specs=[ANY] * 4, out_shape=[out] * 4,
        input_output_aliases={4 + k: k for k in range(len(held))},
        scratch_shapes=[pltpu.VMEM((2, n + 8, c), F32), pltpu.VMEM((2, 3, n, c), F32), pltpu.VMEM((2, 4, n, c), F32),
                        pltpu.SemaphoreType.DMA((2, 4)), pltpu.SemaphoreType.DMA((2, 4))],
        compiler_params=pltpu.CompilerParams(vmem_limit_bytes=VMEM_LIMIT),
    )(g, w3, m3, v3, *held)


def kernel(x, c, ctx, c_ctx, w_mod, b_mod, norm_pre1, norm_post1, norm_pre2, norm_post2, w_in, hg_lb, hg_onorm, gla_w_gk, gla_b_gk, gla_onorm, w_br_hg, w_br_gla, w_out, w_ff_gate, w_ff_up, w_ff_down, loss_target, m_c_ctx, m_w_mod, m_b_mod, m_norm_pre1, m_norm_post1, m_norm_pre2, m_norm_post2, m_w_in, m_hg_lb, m_hg_onorm, m_gla_w_gk, m_gla_b_gk, m_gla_onorm, m_w_br_hg, m_w_br_gla, m_w_out, m_w_ff_gate, m_w_ff_up, m_w_ff_down, v_c_ctx, v_w_mod, v_b_mod, v_norm_pre1, v_norm_post1, v_norm_pre2, v_norm_post2, v_w_in, v_hg_lb, v_hg_onorm, v_gla_w_gk, v_gla_b_gk, v_gla_onorm, v_w_br_hg, v_w_br_gla, v_w_out, v_w_ff_gate, v_w_ff_up, v_w_ff_down):
    xi, yi, ci = lax.axis_index("x"), lax.axis_index("y"), lax.axis_index("c")
    me = 4 * xi + 2 * yi + ci
    t = CTX + x.shape[1]

    w_in_pieces, w_in_state = [], {}

    def w_in_piece(i):
        return (_view_near_rows((i * W_IN_PIECE, (i + 1) * W_IN_PIECE)), w_in_state["src"], w_in_state["land"])

    def started_w_in(handle):
        w_in_state.update(src=handle["srcs"], land=handle["lands"])
        w_in_pieces.append(handle)

    tr_ = lambda a: jnp.swapaxes(a[0], 0, 1)
    w_in_bf = jnp.pad(w_in[0].astype(BF16), ((0, 0), (0, W_IN_PAD - W_IN_SHARD)))
    w_in_state.update(src=[w_in_bf], land=[lax.empty((N_DEV,) + w_in_bf.shape, BF16)])
    gathered = lambda arrs: [(N_DEV,) + a.shape for a in arrs]
    whole = lambda arrs: (_view_whole, arrs, gathered(arrs))
    small_in = [c, hg_lb, gla_w_gk[0], gla_b_gk[0]]
    (small_handle, piece), tok = _split_start([whole(small_in), w_in_piece(0)], "ag_small_start", c)
    started_w_in(piece)
    c_all, lb_g, wgk_g, bgk_g = _split_wait(small_handle, "ag_small_wait", tok)
    big = [w_in[0], w_br_hg[0], w_br_gla[0], w_out[0], tr_(w_ff_gate), tr_(w_ff_up), w_ff_down[0]]
    big_bf = [None] + [w.astype(BF16) for w in big[1:]]
    cols = lambda g: jnp.transpose(g, (1, 0, 2)).reshape(g.shape[1], N_DEV * g.shape[2])

    def get_w_in(after):
        w_full, first = None, 0
        for s, last in enumerate(W_IN_STAGES):
            for i in range(first, last):
                land = _split_wait(w_in_pieces[i], "ag_w_in_wait%d" % i, after if w_full is None else [after, w_full],
                                   srcs=w_in_state["src"], lands=w_in_state["land"])
                w_in_state.update(src=w_in_pieces[i]["srcs"], land=land)
            rows = (first * W_IN_PIECE, last * W_IN_PIECE)
            crossed = _forward_diagonal(w_in_state["land"][0], "ag_w_in_diagonal%d" % s, rows)
            w_in_state["land"] = [_forward_to_sibling(crossed, "ag_w_in_forward%d" % s, rows)]
            w_full = _assemble_w_in(w_in_state["land"][0], rows, w_full, "assemble_w_in%d" % s)
            first = last
        handles, tok = _split_start([whole(big_bf[1:4]), whole(big_bf[4:6]), whole(big_bf[6:])], "ag_big_start",
                                    w_in_state["land"][0])
        big_handles.update(mix=handles[0], ffn=handles[1], down=handles[2])
        return _tie(w_full, tok, "tie_big")

    big_handles = {}

    def get_mix(after):
        g_brh, g_brg, g_out = _split_wait(big_handles["mix"], "ag_mix_wait", after)
        return _gate_cols(cols(g_brh)), _gate_cols(cols(g_brg)), _gate_rows(g_out.reshape(D, D))

    def get_ffn(after):
        g_gate, g_up = _split_wait(big_handles["ffn"], "ag_ffn_wait", after)

        def get_down(after):
            g_down, = _split_wait(big_handles["down"], "ag_down_wait", after)
            return g_down.reshape(D_FF, D)

        return (g_gate.reshape(D_FF, D), g_up.reshape(D_FF, D)), get_down

    hg_lb_full = jnp.transpose(lb_g, (1, 2, 0, 3)).reshape(2, 2, HW)
    wgk_k = _layout_wgk(jnp.transpose(wgk_g, (1, 2, 0, 3)).reshape(2, 16, HW)).astype(BF16)
    bgk_k = jnp.transpose(bgk_g, (1, 0, 2)).reshape(1, D)
    onw = jnp.concatenate([jnp.tile(hg_onorm, (1, NH // 2)), jnp.tile(gla_onorm, (1, NH // 2))], axis=1)

    n_mod = w_mod.shape[2]
    a9 = jnp.concatenate([c_ctx[None], c_all[:, 0], jnp.zeros((16 - 1 - N_DEV, D), F32)], axis=0)
    b_loc = lax.dynamic_slice(b_mod, (0, me * n_mod), (1, n_mod))
    s_loc = _mod_fwd(a9, w_mod[0], b_loc)
    (mod_handle, piece), tok = _split_start([whole([s_loc]), w_in_piece(1)], "ag_mod_start", s_loc)
    started_w_in(piece)
    for i in range(2, D // W_IN_PIECE):
        (piece,), tok = _split_start([w_in_piece(i)], "ag_w_in_start%d" % i, tok)
        started_w_in(piece)
    s_all, = _split_wait(mod_handle, "ag_mod_wait", tok)
    mod_all = jnp.transpose(s_all, (1, 0, 2)).reshape(16, N_DEV * n_mod)
    pad8 = lambda m: jnp.concatenate([m.reshape(6, D), jnp.zeros((2, D), F32)], axis=0)
    modc = pad8(mod_all[0])
    modx = pad8(lax.dynamic_slice(mod_all, (1 + me, 0), (1, N_DEV * n_mod))[0])

    z = (ctx[0], x[0])
    modx = _tie(modx, tok, "tie_mod")
    norms = (norm_pre1, norm_post1, norm_pre2, norm_post2)
    rowshard = lambda d: d.reshape(N_DEV, d.shape[0] // N_DEV, d.shape[1]).astype(BF16)
    sent, w_in_grad = [], {}

    def w_in_chunk(i):
        half, rows = W_IN_GRAD_CHUNKS[i]
        return (_view_window(rows), w_in_grad[half], [(N_DEV, rows[1] - rows[0], D)])

    def sent_w_in(i, handle):
        w_in_grad[W_IN_GRAD_CHUNKS[i][0]] = handle["srcs"]
        sent.append(("w_in%d" % i, ["w_in#%d" % i], handle))

    def send(names, grads, x_after):
        if names == ("w_in_a",):
            w_in_grad["a"] = list(grads)
            (handle,), tok = _split_start([w_in_chunk(0)], "grads_w_in0_start", x_after)
            sent_w_in(0, handle)
            return _tie(x_after, tok, "tie_w_in0")
        if names == ("w_in_b",):
            w_in_grad["b"] = list(grads)
            return x_after
        arrs, leaves, col_arrs, col_leaves = [], [], [], []
        for nm, g in zip(names, grads):
            if nm in ("w_gate_t", "w_up_t"):
                arrs.append(rowshard(g))
                leaves.append({"w_gate_t": "w_ff_gate", "w_up_t": "w_ff_up"}[nm])
            elif nm == "w_down":
                arrs.append(rowshard(g))
                leaves.append("w_ff_down")
            elif nm == "w_out":
                arrs.append(rowshard(g[GOFF:GOFF + D]))
                leaves.append(nm)
            else:
                col_arrs.append(g[:, GOFF:GOFF + D])
                col_leaves.append(nm)
        groups = [(_view_block, arrs, [a.shape for a in arrs])]
        if col_arrs:
            groups.append((_view_cols, col_arrs, [(N_DEV, a.shape[0], D // N_DEV) for a in col_arrs]))
        handles, tok = _split_start(groups, "grads_%s_start" % names[0], x_after)
        sent.append((names[0], leaves, handles[0]))
        if col_arrs:
            sent.append((names[0] + "_cols", col_leaves, handles[1]))
        return _tie(x_after, tok, "tie_" + names[0])

    r = _local_step(z, loss_target[0], modc, modx, norms, onw, hg_lb_full, wgk_k, bgk_k,
                    get_w_in, get_mix, get_ffn, send)
    grad_x = r["grad_x"][None]

    sm_pre, sm_mid, sm_fin = r["sm_pre"], r["sm_mid"], r["sm_final"]
    dmodc = jnp.stack([sm_pre[0], sm_pre[2], sm_mid[4], sm_mid[0], sm_mid[2], sm_fin[0]]).reshape(-1)
    dmodx = jnp.stack([sm_pre[1], sm_pre[3], sm_mid[5], sm_mid[1], sm_mid[3], sm_fin[1]]).reshape(-1)
    on = r["sm_post"][0].reshape(NH, DH)
    pieces = [dmodc, dmodx, sm_pre[4], sm_mid[7], sm_mid[6], sm_fin[2], on[:NH // 2].sum(0), on[NH // 2:].sum(0),
              r["d_lb"][:2].reshape(-1), _unlayout_wgk(r["d_wgk"]).reshape(-1), r["d_bgk"][0]]
    loss_local = (0.5 / D) * jnp.sum(r["loss_vec"])
    pieces.append(jnp.concatenate([loss_local.reshape(1), jnp.zeros((DH - 1,), F32)]))
    sizes = [p.shape[0] for p in pieces]
    pack = jnp.concatenate(pieces).reshape(-1, DH)
    moms = [(m_w_in, v_w_in), (m_w_br_hg, v_w_br_hg), (m_w_br_gla, v_w_br_gla), (m_w_out, v_w_out),
            (m_w_ff_gate, v_w_ff_gate), (m_w_ff_up, v_w_ff_up), (m_w_ff_down, v_w_ff_down)]
    names = ["w_in", "w_br_hg", "w_br_gla", "w_out", "w_ff_gate", "w_ff_up", "w_ff_down"]
    wmv = {nm: (w, m, v) for nm, w, (m, v) in zip(names, big, moms)}
    res, updated = {}, {}

    def update(nm):
        w, m, v = wmv[nm]
        if nm in ("w_ff_gate", "w_ff_up"):
            outs = _adamw(recv[nm], w, tr_(m), tr_(v), "adamw_" + nm)
            res[nm] = [jnp.swapaxes(o, 0, 1)[None] for o in outs]
        else:
            outs = _adamw(recv[nm], w, m[0], v[0], "adamw_" + nm)
            res[nm] = [o[None] for o in outs]
        updated[nm] = outs[0]

    (small_handle, handle), tok = _split_start([whole([pack]), w_in_chunk(1)], "small_grads_start", pack)
    sent_w_in(1, handle)
    recv = {}
    for first, leaves, handle in sent:
        if not first.startswith("w_in"):
            recv.update(zip(leaves, _split_wait(handle, "grads_%s_wait" % first, tok)))
    update("w_ff_gate")
    update("w_ff_up")
    pack_all, = _split_wait(small_handle, "small_grads_wait", [updated["w_ff_gate"], updated["w_ff_up"]])
    tot = _sum_devices(pack_all).reshape(-1)
    offs = [sum(sizes[:i]) for i in range(len(sizes))]
    part = lambda i: tot[offs[i]:offs[i] + sizes[i]]
    dmodc_t, dmodx_t = part(0), part(1)
    g_b_mod = (dmodc_t + dmodx_t)[None]
    g_norms = [part(i)[None] for i in (2, 3, 4, 5)]
    g_hg_on, g_gla_on = part(6)[None], part(7)[None]
    lb0 = lax.dynamic_slice(part(8).reshape(2, HW), (0, me * (HW // N_DEV)), (2, HW // N_DEV))
    g_hg_lb = jnp.stack([lb0, -lb0])
    g_wgk = lax.dynamic_slice(part(9).reshape(2, 16, HW), (0, 0, me * (HW // N_DEV)), (2, 16, HW // N_DEV))[None]
    g_bgk = lax.dynamic_slice(part(10).reshape(2, HW), (0, me * (HW // N_DEV)), (2, HW // N_DEV))[None]
    loss = part(11)[0]

    dmx_all = pack_all.reshape(N_DEV, -1)[:, sizes[0]:sizes[0] + sizes[1]]
    d9 = jnp.concatenate([lax.dynamic_slice(dmodc_t[None], (0, me * n_mod), (1, n_mod)),
                          lax.dynamic_slice(dmx_all, (0, me * n_mod), (N_DEV, n_mod)),
                          jnp.zeros((16 - 1 - N_DEV, n_mod), F32)], axis=0)
    g_w_mod, dcc_part = _mod_bwd(a9, d9, w_mod[0])
    (cctx_handle, handle), tok = _split_start([whole([dcc_part]), w_in_chunk(2)], "c_ctx_start", dcc_part)
    sent_w_in(2, handle)
    recv["w_ff_down"] = _tie(recv["w_ff_down"], tok, "tie_down")
    update("w_ff_down")
    res["w_mod"] = [o[None] for o in _adamw(g_w_mod[None], w_mod[0], m_w_mod[0], v_w_mod[0], "adamw_w_mod")]
    for nm in ("w_out", "w_br_hg", "w_br_gla"):
        update(nm)
    dcc_all, = _split_wait(cctx_handle, "c_ctx_wait", [updated[nm] for nm in names[1:]] + [res["w_mod"][0]])
    g_c_ctx = _sum_devices(dcc_all)[0]

    small = [("c_ctx", c_ctx, m_c_ctx, v_c_ctx, g_c_ctx), ("b_mod", b_mod, m_b_mod, v_b_mod, g_b_mod),
             ("norm_pre1", norm_pre1, m_norm_pre1, v_norm_pre1, g_norms[0]),
             ("norm_post1", norm_post1, m_norm_post1, v_norm_post1, g_norms[1]),
             ("norm_pre2", norm_pre2, m_norm_pre2, v_norm_pre2, g_norms[2]),
             ("norm_post2", norm_post2, m_norm_post2, v_norm_post2, g_norms[3]),
             ("hg_lb", hg_lb, m_hg_lb, v_hg_lb, g_hg_lb), ("hg_onorm", hg_onorm, m_hg_onorm, v_hg_onorm, g_hg_on),
             ("gla_w_gk", gla_w_gk, m_gla_w_gk, v_gla_w_gk, g_wgk), ("gla_b_gk", gla_b_gk, m_gla_b_gk, v_gla_b_gk, g_bgk),
             ("gla_onorm", gla_onorm, m_gla_onorm, v_gla_onorm, g_gla_on)]
    flat = lambda k: jnp.concatenate([s[k].reshape(-1) for s in small]).reshape(-1, DH)
    outs = _adamw(flat(4)[None], flat(1), flat(2), flat(3), "adamw_small")
    off = 0
    for nm, w, _, _, _ in small:
        res[nm] = [o.reshape(-1)[off:off + w.size].reshape(w.shape) for o in outs]
        off += w.size

    done = [updated[nm] for nm in names[1:]] + [res["w_mod"][0]] + [o for nm, *_ in small for o in res[nm]]
    major = lambda a: jnp.transpose(a, (2, 0, 1))
    outs, row0 = None, 0
    for i, (first, leaves, handle) in enumerate(s for s in sent if s[0].startswith("w_in")):
        half = W_IN_GRAD_CHUNKS[i][0]
        land, = _split_wait(handle, "grads_%s_wait" % first, done, srcs=w_in_grad[half])
        w_in_grad[half] = handle["srcs"]
        rows = (row0, row0 + land.shape[1])
        outs = _adamw_rows3(_sum_windows(land, "sum_windows%d" % i), major(w_in), major(m_w_in), major(v_w_in),
                            "adamw_w_in%d" % i, rows, outs)
        row0 = rows[1]
    res["w_in"] = [jnp.transpose(o, (1, 2, 0)) for o in outs]

    order = ["c_ctx", "w_mod", "b_mod", "norm_pre1", "norm_post1", "norm_pre2", "norm_post2", "w_in", "hg_lb",
             "hg_onorm", "gla_w_gk", "gla_b_gk", "gla_onorm", "w_br_hg", "w_br_gla", "w_out", "w_ff_gate", "w_ff_up",
             "w_ff_down"]
    return (loss, grad_x, *[res[n][k] for k in range(4) for n in order])
```

```python
import functools

import jax
import jax.numpy as jnp
from jax import lax
from jax.experimental import pallas as pl
from jax.experimental.pallas import tpu as pltpu

F32 = jnp.float32
BF16 = jnp.bfloat16
HI = lax.Precision.HIGHEST

N_DEV = 8
D = 1024
CTX = 256
HW = 512
DH = 128
NH = 8
D_FF = 2816
EPS = 1e-6
GLA_NORM = 16.0
CHUNK = 64
TR = 256
NCT = CTX // TR
W_IN_COLS = 7168
MAIN0 = 0
LR0 = 4608
GW = 1152
GOFF = 32
GATE_HG0 = LR0
GATE_GLA0 = LR0 + D
LEVELS = (32, 16, 8)
EXP_CLAMP = 80.0
VMEM_LIMIT = 48 * 1024 * 1024

ADAM_LR, ADAM_B1, ADAM_B2, ADAM_EPS, ADAM_WD, ADAM_STEP = 0.001, 0.9, 0.999, 1e-08, 0.01, 10


def _cp(*sem):
    return pltpu.CompilerParams(dimension_semantics=sem, vmem_limit_bytes=VMEM_LIMIT)


def _sig(x):
    return jax.nn.sigmoid(x)


def _silu(x):
    return x * _sig(x)


def _dsilu(x):
    s = _sig(x)
    return s * (1.0 + x * (1.0 - s))


def _rstd(x):
    return lax.rsqrt(jnp.mean(x * x, axis=-1, keepdims=True) + EPS)


def _rms_bwd(a, y, r):
    return r * (a - y * (r * r) * jnp.mean(a * y, axis=-1, keepdims=True))


def _colsum(x):
    return jnp.sum(x, axis=0, keepdims=True)


def _dot(a, b, dims, precision=None):
    return lax.dot_general(a, b, (dims, ((), ())), preferred_element_type=F32, precision=precision)


NN = ((1,), (0,))
NT = ((1,), (1,))
TN = ((0,), (0,))

SCAN_HEADS_FWD = 4
SCAN_HEADS_BWD = 4


def _split_dot(m, x):
    mb = m.astype(BF16)
    x1 = x.astype(BF16)
    r1 = x - x1.astype(F32)
    x2 = r1.astype(BF16)
    x3 = (r1 - x2.astype(F32)).astype(BF16)
    return _dot(mb, x1, NN) + _dot(mb, x2, NN) + _dot(mb, x3, NN)


def _matmul(a, b, dims, out_dtype, name, tm, tn, tk, a_off=0, m_out=None):
    a_pair = isinstance(a, (tuple, list))
    as_ = list(a) if a_pair else [a]
    a = as_[0]
    pair = isinstance(b, (tuple, list))
    bs = list(b) if pair else [b]
    b1 = bs[0]
    rows = b1.shape[0] * len(bs)
    half = None
    if dims == NN:
        m, k, n = a.shape[0], rows, b1.shape[1]
        a_spec = pl.BlockSpec((tm, tk), lambda i, j, kk: (i, kk + a_off))
        half = b1.shape[0] // tk
        if a_pair:
            assert pair and a.shape[1] == b1.shape[0] and a_off == 0
            a_spec = [pl.BlockSpec((tm, tk), lambda i, j, kk: (i, jnp.minimum(kk, half - 1))),
                      pl.BlockSpec((tm, tk), lambda i, j, kk: (i, jnp.maximum(kk - half, 0)))]
        b_maps = [lambda i, j, kk: (kk, j)] if not pair else [
            lambda i, j, kk: (jnp.minimum(kk, half - 1), j), lambda i, j, kk: (jnp.maximum(kk - half, 0), j)]
        b_specs = [pl.BlockSpec((tk, tn), f) for f in b_maps]
        axis = 2
    elif dims == NT:
        m, k, n = a.shape[0], b1.shape[1], rows
        a_spec = pl.BlockSpec((tm, tk), lambda i, j, kk: (i, kk + a_off))
        half = b1.shape[0] // tn
        b_maps = [lambda i, j, kk: (j, kk)] if not pair else [
            lambda i, j, kk: (jnp.minimum(j, half - 1), kk), lambda i, j, kk: (jnp.maximum(j - half, 0), kk)]
        b_specs = [pl.BlockSpec((tn, tk), f) for f in b_maps]
        axis = 1
    else:
        assert not pair
        m, k = (a.shape[1] if m_out is None else m_out), a.shape[0]
        n = b1.shape[1]
        a_spec = pl.BlockSpec((tk, tm), lambda i, j, kk: (kk, i + a_off))
        b_specs = [pl.BlockSpec((tk, tn), lambda i, j, kk: (kk, j))]
    assert m % tm == 0 and n % tn == 0 and k % tk == 0, (name, m, n, k, tm, tn, tk)
    nk = k // tk
    nb = len(bs)
    na = len(as_)
    assert na == 1 or dims == NN

    def body(*refs):
        a_refs, refs = refs[:na], refs[na:]
        o_ref = refs[nb]
        if pair:
            bv = jnp.where(pl.program_id(axis) < half, refs[0][...], refs[1][...])
        else:
            bv = refs[0][...]
        av = a_refs[0][...] if na == 1 else jnp.where(pl.program_id(2) < half, a_refs[0][...], a_refs[1][...])
        part = _dot(av, bv, dims)
        if nk == 1:
            o_ref[...] = part.astype(o_ref.dtype)
            return
        acc_ref = refs[nb + 1]
        kk = pl.program_id(2)

        @pl.when(kk == 0)
        def _():
            acc_ref[...] = part

        @pl.when(kk > 0)
        def _():
            acc_ref[...] += part

        @pl.when(kk == nk - 1)
        def _():
            o_ref[...] = acc_ref[...].astype(o_ref.dtype)

    return pl.pallas_call(
        body,
        name=name,
        grid=(m // tm, n // tn, nk),
        in_specs=(a_spec if a_pair else [a_spec]) + b_specs,
        out_specs=pl.BlockSpec((tm, tn), lambda i, j, kk: (i, j)),
        out_shape=jax.ShapeDtypeStruct((m, n), out_dtype),
        scratch_shapes=[] if nk == 1 else [pltpu.VMEM((tm, tn), F32)],
        compiler_params=_cp("parallel", "parallel", "arbitrary"),
    )(*as_, *bs)


def _mm_gu_act(h, w_gate_t, w_up_t, name, tm):
    t = h.shape[0]
    tn = D_FF // 2

    def body(a_ref, bg_ref, bu_ref, u_ref, v_ref, act_ref):
        a = a_ref[...]
        u = _dot(a, bg_ref[...], NT)
        v = _dot(a, bu_ref[...], NT)
        u_ref[...] = u.astype(BF16)
        v_ref[...] = v.astype(BF16)
        act_ref[...] = (_silu(u) * v).astype(BF16)

    wspec = pl.BlockSpec((tn, D), lambda i, j: (j, 0))
    ospec = pl.BlockSpec((tm, tn), lambda i, j: (i, j))
    out = jax.ShapeDtypeStruct((t, D_FF), BF16)
    return pl.pallas_call(
        body, name=name, grid=(t // tm, D_FF // tn),
        in_specs=[pl.BlockSpec((tm, D), lambda i, j: (i, 0)), wspec, wspec],
        out_specs=[ospec] * 3, out_shape=[out] * 3,
        compiler_params=_cp("parallel", "parallel"),
    )(h, w_gate_t, w_up_t)


def _mm_down_dx_act(dy, w_down, u, v, name, tm):
    t = dy.shape[0]
    tn = D_FF // 2

    def body(a_ref, b_ref, u_ref, v_ref, du_ref, dv_ref):
        dact = _dot(a_ref[...], b_ref[...], NT)
        u = u_ref[...].astype(F32)
        du_ref[...] = (dact * v_ref[...].astype(F32) * _dsilu(u)).astype(BF16)
        dv_ref[...] = (dact * _silu(u)).astype(BF16)

    ospec = pl.BlockSpec((tm, tn), lambda i, j: (i, j))
    out = jax.ShapeDtypeStruct((t, D_FF), BF16)
    return pl.pallas_call(
        body, name=name, grid=(t // tm, D_FF // tn),
        in_specs=[pl.BlockSpec((tm, D), lambda i, j: (i, 0)), pl.BlockSpec((tn, D), lambda i, j: (j, 0)), ospec, ospec],
        out_specs=[ospec] * 2, out_shape=[out] * 2,
        compiler_params=_cp("parallel", "parallel"),
    )(dy, w_down, u, v)


def _row(c):
    return pl.BlockSpec((TR, c), lambda i: (i, 0))


def _rowcol(width, cb):
    return pl.BlockSpec((TR, width), lambda i: (i, cb))


def _full(shape):
    return pl.BlockSpec(shape, lambda i: (0,) * len(shape))


def _mod_row(mc_ref, mx_ref, k, is_ctx):
    return jnp.where(is_ctx, mc_ref[k:k + 1, :], mx_ref[k:k + 1, :])


def _z_specs():
    return [pl.BlockSpec((TR, D), lambda i: (jnp.minimum(i, NCT - 1), 0)),
            pl.BlockSpec((TR, D), lambda i: (jnp.maximum(i - NCT, 0), 0))]


def _z_tile(c_ref, x_ref, is_ctx):
    return jnp.where(is_ctx, c_ref[...], x_ref[...])


def _acc_row(ref, k, val):
    ref[k:k + 1, :] += val


def _acc_mod(ref, k, is_ctx, val):
    zero = jnp.zeros_like(val)
    ref[k:k + 1, :] += jnp.where(is_ctx, val, zero)
    ref[k + 1:k + 2, :] += jnp.where(is_ctx, zero, val)


def _prenorm(z, nw, modc, modx, i_shift, i_scale, name):
    t = z[0].shape[0] + z[1].shape[0]

    def body(zc_ref, zx_ref, nw_ref, mc_ref, mx_ref, h_ref):
        is_ctx = pl.program_id(0) < NCT
        x = _z_tile(zc_ref, zx_ref, is_ctx)
        n = x * _rstd(x) * nw_ref[...]
        h = n * (1.0 + _mod_row(mc_ref, mx_ref, i_scale, is_ctx)) + _mod_row(mc_ref, mx_ref, i_shift, is_ctx)
        h_ref[...] = h.astype(BF16)

    return pl.pallas_call(
        body, name=name, grid=(t // TR,),
        in_specs=_z_specs() + [_full((1, D)), _full((8, D)), _full((8, D))],
        out_specs=_row(D),
        out_shape=jax.ShapeDtypeStruct((t, D), BF16),
        compiler_params=_cp("parallel"),
    )(*z, nw, modc, modx)


def _hg_lb(lb_ref, d):
    a0 = lb_ref[0, d:d + 1, :]
    a1 = lb_ref[1, d:d + 1, :]
    mx = jnp.maximum(a0, a1)
    e0 = jnp.exp(a0 - mx)
    e1 = jnp.exp(a1 - mx)
    return e0 / (e0 + e1)


def _log_sigmoid(x):
    return jnp.minimum(x, 0.0) - jnp.log(1.0 + jnp.exp(-jnp.abs(x)))


def _gates_fwd(p, hg_lb, wgk, bgk):
    t = p.shape[0]
    seg = lambda j: _rowcol(HW, MAIN0 // HW + j)

    def body(hq_ref, hi_ref, hf_ref, hb_ref, gq_ref, gk_ref, gv_ref, lr_ref, lb_ref, wgk_ref, bgk_ref,
             q_ref, v_ref, kf_ref, kb_ref, gf_ref, gb_ref):
        q_ref[:, :HW] = _silu(hq_ref[...].astype(F32)).astype(BF16)
        q_ref[:, HW:] = (gq_ref[...].astype(F32) * (DH ** -0.5)).astype(BF16)
        v_ref[:, :HW] = hi_ref[...]
        v_ref[:, HW:] = gv_ref[...]
        xg = _dot(lr_ref[...].astype(BF16), wgk_ref[...], NN) + bgk_ref[...]
        for d, (raw_ref, k_ref, g_ref) in enumerate(((hf_ref, kf_ref, gf_ref), (hb_ref, kb_ref, gb_ref))):
            lbd = _hg_lb(lb_ref, d)
            f = lbd + (1.0 - lbd) * _sig(raw_ref[...].astype(F32))
            k_ref[:, :HW] = (1.0 - f).astype(BF16)
            k_ref[:, HW:] = gk_ref[...]
            g_ref[:, :HW] = jnp.log(f)
            g_ref[:, HW:] = _log_sigmoid(xg[:, d * HW:(d + 1) * HW]) * (1.0 / GLA_NORM)

    out = jax.ShapeDtypeStruct((t, D), F32)
    outb = jax.ShapeDtypeStruct((t, D), BF16)
    return pl.pallas_call(
        body, name="gates_fwd", grid=(t // TR,),
        in_specs=[seg(0), seg(1), seg(2), seg(3), seg(5), seg(6), seg(7), _rowcol(DH, LR0 // DH),
                  _full((2, 2, HW)), _full((DH, D)), _full((1, D))],
        out_specs=[_row(D)] * 6,
        out_shape=[outb] * 4 + [out] * 2,
        compiler_params=_cp("parallel"),
    )(p, p, p, p, p, p, p, p, hg_lb, wgk, bgk)


def _post_fwd(o_fw, o_bw, p, onw):
    t = o_fw.shape[0]

    def body(of_ref, ob_ref, g1_ref, g2_ref, w_ref, y_ref):
        for h in range(NH):
            sl = slice(h * DH, (h + 1) * DH)
            o = of_ref[:, sl] + ob_ref[:, sl]
            g_ref = g1_ref if h < NH // 2 else g2_ref
            gs = slice((h % (NH // 2)) * DH, (h % (NH // 2) + 1) * DH)
            n = o * _rstd(o) * w_ref[:, sl]
            y_ref[:, sl] = (n * _silu(g_ref[:, gs].astype(F32))).astype(BF16)

    return pl.pallas_call(
        body, name="post_fwd", grid=(t // TR,),
        in_specs=[_row(D), _row(D), _rowcol(HW, MAIN0 // HW + 4), _rowcol(HW, MAIN0 // HW + 8), _full((1, D))],
        out_specs=_row(D),
        out_shape=jax.ShapeDtypeStruct((t, D), BF16),
        compiler_params=_cp("parallel"),
    )(o_fw, o_bw, p, p, onw)


def _gate_window_specs(col0):
    return [_rowcol(HW, col0 // HW), _rowcol(HW, col0 // HW + 1), _rowcol(DH, (col0 + 2 * HW) // DH)]


def _gate_window(refs):
    return jnp.concatenate([r[...].astype(F32) for r in refs], axis=1)


def _branch_merge(y, w_hg, w_gla, p):
    t = y.shape[0]

    def body(y_ref, wh_ref, wg_ref, a0, a1, a2, b0, b1, b2, u1_ref, u2_ref, m_ref):
        u1 = _dot(y_ref[:, :HW], wh_ref[...], NN)
        u2 = _dot(y_ref[:, HW:], wg_ref[...], NN)
        u1_ref[...] = u1.astype(BF16)
        u2_ref[...] = u2.astype(BF16)
        m_ref[...] = (_sig(_gate_window((a0, a1, a2))) * u1 + _sig(_gate_window((b0, b1, b2))) * u2).astype(BF16)

    out = jax.ShapeDtypeStruct((t, GW), BF16)
    return pl.pallas_call(
        body, name="branch_merge", grid=(t // TR,),
        in_specs=[_row(D), _full((HW, GW)), _full((HW, GW))] + _gate_window_specs(GATE_HG0)
        + _gate_window_specs(GATE_GLA0),
        out_specs=[_row(GW)] * 3, out_shape=[out] * 3,
        compiler_params=_cp("parallel"),
    )(y, w_hg, w_gla, p, p, p, p, p, p)


def _mid_fwd(z, y1, nw_post, nw_pre, modc, modx):
    t = y1.shape[0]

    def body(zc_ref, zx_ref, y_ref, wpo_ref, wpr_ref, mc_ref, mx_ref, z1_ref, h_ref):
        is_ctx = pl.program_id(0) < NCT
        y = y_ref[...].astype(F32)
        z1 = _z_tile(zc_ref, zx_ref, is_ctx) + _mod_row(mc_ref, mx_ref, 2, is_ctx) * (y * _rstd(y) * wpo_ref[...])
        z1_ref[...] = z1
        n = z1 * _rstd(z1) * wpr_ref[...]
        h = n * (1.0 + _mod_row(mc_ref, mx_ref, 4, is_ctx)) + _mod_row(mc_ref, mx_ref, 3, is_ctx)
        h_ref[...] = h.astype(BF16)

    return pl.pallas_call(
        body, name="mid_fwd", grid=(t // TR,),
        in_specs=_z_specs() + [_row(D), _full((1, D)), _full((1, D)), _full((8, D)), _full((8, D))],
        out_specs=[_row(D), _row(D)],
        out_shape=[jax.ShapeDtypeStruct((t, D), F32), jax.ShapeDtypeStruct((t, D), BF16)],
        compiler_params=_cp("parallel"),
    )(*z, y1, nw_post, nw_pre, modc, modx)


def _final(z1, y2, target, nw, modc, modx):
    t = z1.shape[0]

    def body(z1_ref, y_ref, tg_ref, w_ref, mc_ref, mx_ref, dz_ref, dy_ref, loss_ref, sm_ref):
        i = pl.program_id(0)
        is_ctx = i < NCT

        @pl.when(i == 0)
        def _():
            loss_ref[...] = jnp.zeros_like(loss_ref)
            sm_ref[...] = jnp.zeros_like(sm_ref)

        g = _mod_row(mc_ref, mx_ref, 5, is_ctx)
        y = y_ref[...].astype(F32)
        r = _rstd(y)
        w = w_ref[...]
        yr = y * r
        n = yr * w
        e = z1_ref[...] + g * n - tg_ref[...]
        lat = jnp.where(is_ctx, 0.0, 1.0)
        loss_ref[...] += lat * _colsum(e * e)
        dz = e * (lat / D)
        dz_ref[...] = dz
        _acc_mod(sm_ref, 0, is_ctx, _colsum(dz * n))
        dn = dz * g
        _acc_row(sm_ref, 2, _colsum(dn * yr))
        dy_ref[...] = _rms_bwd(dn * w, y, r).astype(BF16)

    return pl.pallas_call(
        body, name="final", grid=(t // TR,),
        in_specs=[_row(D), _row(D), pl.BlockSpec((TR, D), lambda i: (jnp.maximum(i - NCT, 0), 0)),
                  _full((1, D)), _full((8, D)), _full((8, D))],
        out_specs=[_row(D), _row(D), _full((1, D)), _full((8, D))],
        out_shape=[jax.ShapeDtypeStruct((t, D), F32), jax.ShapeDtypeStruct((t, D), BF16),
                   jax.ShapeDtypeStruct((1, D), F32), jax.ShapeDtypeStruct((8, D), F32)],
        compiler_params=_cp("arbitrary"),
    )(z1, y2, target, nw, modc, modx)


def _mid_bwd(dh2, dz, z1, y1, nw_post, nw_pre, modc, modx):
    t = z1.shape[0]

    def body(dh_ref, dz_ref, z1_ref, y_ref, wpo_ref, wpr_ref, mc_ref, mx_ref, dzo_ref, dy_ref, sm_ref):
        i = pl.program_id(0)
        is_ctx = i < NCT

        @pl.when(i == 0)
        def _():
            sm_ref[...] = jnp.zeros_like(sm_ref)

        dh = dh_ref[...].astype(F32)
        z1 = z1_ref[...]
        r = _rstd(z1)
        zr = z1 * r
        wpr = wpr_ref[...]
        n = zr * wpr
        _acc_mod(sm_ref, 0, is_ctx, _colsum(dh))
        _acc_mod(sm_ref, 2, is_ctx, _colsum(dh * n))
        dn = dh * (1.0 + _mod_row(mc_ref, mx_ref, 4, is_ctx))
        _acc_row(sm_ref, 6, _colsum(dn * zr))
        dz1 = dz_ref[...] + _rms_bwd(dn * wpr, z1, r)
        dzo_ref[...] = dz1
        y = y_ref[...].astype(F32)
        r1 = _rstd(y)
        yr = y * r1
        wpo = wpo_ref[...]
        g = _mod_row(mc_ref, mx_ref, 2, is_ctx)
        _acc_mod(sm_ref, 4, is_ctx, _colsum(dz1 * (yr * wpo)))
        dn1 = dz1 * g
        _acc_row(sm_ref, 7, _colsum(dn1 * yr))
        dy_ref[...] = _rms_bwd(dn1 * wpo, y, r1).astype(BF16)

    return pl.pallas_call(
        body, name="mid_bwd", grid=(t // TR,),
        in_specs=[_row(D)] * 4 + [_full((1, D)), _full((1, D)), _full((8, D)), _full((8, D))],
        out_specs=[_row(D), _row(D), _full((8, D))],
        out_shape=[jax.ShapeDtypeStruct((t, D), F32), jax.ShapeDtypeStruct((t, D), BF16),
                   jax.ShapeDtypeStruct((8, D), F32)],
        compiler_params=_cp("arbitrary"),
    )(dh2, dz, z1, y1, nw_post, nw_pre, modc, modx)


def _pre_bwd(dh1, dz, z, nw, modc, modx):
    t = dh1.shape[0]

    def body(dh_ref, dz_ref, zc_ref, zx_ref, w_ref, mc_ref, mx_ref, dzo_ref, sm_ref):
        i = pl.program_id(0)
        is_ctx = i < NCT

        @pl.when(i == 0)
        def _():
            sm_ref[...] = jnp.zeros_like(sm_ref)

        dh = dh_ref[...].astype(F32)
        x = _z_tile(zc_ref, zx_ref, is_ctx)
        r = _rstd(x)
        xr = x * r
        w = w_ref[...]
        _acc_mod(sm_ref, 0, is_ctx, _colsum(dh))
        _acc_mod(sm_ref, 2, is_ctx, _colsum(dh * (xr * w)))
        dn = dh * (1.0 + _mod_row(mc_ref, mx_ref, 1, is_ctx))
        _acc_row(sm_ref, 4, _colsum(dn * xr))
        dzo_ref[...] = dz_ref[...] + _rms_bwd(dn * w, x, r)

    return pl.pallas_call(
        body, name="pre_bwd", grid=(t // TR,),
        in_specs=[_row(D)] * 2 + _z_specs() + [_full((1, D)), _full((8, D)), _full((8, D))],
        out_specs=[pl.BlockSpec((TR, D), lambda i: (jnp.maximum(i - NCT, 0), 0)), _full((8, D))],
        out_shape=[jax.ShapeDtypeStruct((t - CTX, D), F32), jax.ShapeDtypeStruct((8, D), F32)],
        compiler_params=_cp("arbitrary"),
    )(dh1, dz, *z, nw, modc, modx)


def _branch_merge_bwd(dm, p, u1, u2, w_hg, w_gla):
    t = dm.shape[0]

    def body(dm_ref, a0, a1, a2, b0, b1, b2, u1_ref, u2_ref, wh_ref, wg_ref, du1_ref, du2_ref, dg_ref, dyh_ref, dyg_ref):
        dm_ = dm_ref[...].astype(F32)
        s1 = _sig(_gate_window((a0, a1, a2)))
        s2 = _sig(_gate_window((b0, b1, b2)))
        du1 = (dm_ * s1).astype(BF16)
        du2 = (dm_ * s2).astype(BF16)
        du1_ref[...] = du1
        du2_ref[...] = du2
        dg_ref[:, :GW] = (dm_ * u1_ref[...].astype(F32) * s1 * (1.0 - s1)).astype(BF16)
        dg_ref[:, GW:] = (dm_ * u2_ref[...].astype(F32) * s2 * (1.0 - s2)).astype(BF16)
        dyh_ref[...] = _dot(du1, wh_ref[...], NT).astype(BF16)
        dyg_ref[...] = _dot(du2, wg_ref[...], NT).astype(BF16)

    return pl.pallas_call(
        body, name="branch_merge_bwd", grid=(t // TR,),
        in_specs=[_row(GW)] + _gate_window_specs(GATE_HG0) + _gate_window_specs(GATE_GLA0)
        + [_row(GW), _row(GW), _full((HW, GW)), _full((HW, GW))],
        out_specs=[_row(GW), _row(GW), _row(2 * GW), _row(HW), _row(HW)],
        out_shape=[jax.ShapeDtypeStruct((t, GW), BF16), jax.ShapeDtypeStruct((t, GW), BF16),
                   jax.ShapeDtypeStruct((t, 2 * GW), BF16), jax.ShapeDtypeStruct((t, HW), BF16),
                   jax.ShapeDtypeStruct((t, HW), BF16)],
        compiler_params=_cp("parallel"),
    )(dm, p, p, p, p, p, p, u1, u2, w_hg, w_gla)


def _post_bwd(dy_hg, dy_gla, o_fw, o_bw, p, onw):
    t = o_fw.shape[0]

    def body(d1_ref, d2_ref, of_ref, ob_ref, g1_ref, g2_ref, w_ref, do_ref, dg_ref, sm_ref):
        @pl.when(pl.program_id(0) == 0)
        def _():
            sm_ref[...] = jnp.zeros_like(sm_ref)

        for h in range(NH):
            sl = slice(h * DH, (h + 1) * DH)
            gs = slice((h % (NH // 2)) * DH, (h % (NH // 2) + 1) * DH)
            g_ref, d_ref = (g1_ref, d1_ref) if h < NH // 2 else (g2_ref, d2_ref)
            o = of_ref[:, sl] + ob_ref[:, sl]
            r = _rstd(o)
            orr = o * r
            w = w_ref[:, sl]
            gt = g_ref[:, gs].astype(F32)
            dy = d_ref[:, gs].astype(F32)
            dg_ref[:, sl] = (dy * (orr * w) * _dsilu(gt)).astype(BF16)
            dn = dy * _silu(gt)
            sm_ref[0:1, sl] += _colsum(dn * orr)
            do_ref[:, sl] = _rms_bwd(dn * w, o, r)

    return pl.pallas_call(
        body, name="post_bwd", grid=(t // TR,),
        in_specs=[_row(HW), _row(HW), _row(D), _row(D), _rowcol(HW, MAIN0 // HW + 4), _rowcol(HW, MAIN0 // HW + 8),
                  _full((1, D))],
        out_specs=[_row(D), _row(D), _full((8, D))],
        out_shape=[jax.ShapeDtypeStruct((t, D), F32), jax.ShapeDtypeStruct((t, D), BF16),
                   jax.ShapeDtypeStruct((8, D), F32)],
        compiler_params=_cp("arbitrary"),
    )(dy_hg, dy_gla, o_fw, o_bw, p, p, onw)


def _gates_bwd(p, hg_lb, wgk, bgk, dgm, dgo, dq_f, dq_b, dv_f, dv_b, dk_f, dk_b, dg_f, dg_b):
    t = p.shape[0]
    seg = lambda j: _rowcol(HW, MAIN0 // HW + j)

    def body(hq_ref, hf_ref, hb_ref, lr_ref, lb_ref, wgk_ref, bgk_ref, dgm_ref, dgo_ref,
             dqf_ref, dqb_ref, dvf_ref, dvb_ref, dkf_ref, dkb_ref, dgf_ref, dgb_ref,
             dp_ref, dlb_ref, dw_ref, db_ref):
        @pl.when(pl.program_id(0) == 0)
        def _():
            dlb_ref[...] = jnp.zeros_like(dlb_ref)
            dw_ref[...] = jnp.zeros_like(dw_ref)
            db_ref[...] = jnp.zeros_like(db_ref)

        c0 = MAIN0

        def put(j, val):
            dp_ref[:, c0 + j * HW:c0 + (j + 1) * HW] = val.astype(BF16)

        dq = dqf_ref[...].astype(F32) + dqb_ref[...].astype(F32)
        dv = dvf_ref[...].astype(F32) + dvb_ref[...].astype(F32)
        put(0, dq[:, :HW] * _dsilu(hq_ref[...].astype(F32)))
        put(1, dv[:, :HW])
        put(5, dq[:, HW:] * (DH ** -0.5))
        put(7, dv[:, HW:])
        put(6, dkf_ref[:, HW:].astype(F32) + dkb_ref[:, HW:].astype(F32))
        dp_ref[:, c0 + 4 * HW:c0 + 5 * HW] = dgo_ref[:, :HW]
        dp_ref[:, c0 + 8 * HW:c0 + 9 * HW] = dgo_ref[:, HW:]
        lr = lr_ref[...].astype(BF16)
        xg = _dot(lr, wgk_ref[...], NN) + bgk_ref[...]
        dxg = []
        for d, (raw_ref, dk_ref, dg_ref) in enumerate(((hf_ref, dkf_ref, dgf_ref), (hb_ref, dkb_ref, dgb_ref))):
            lbd = _hg_lb(lb_ref, d)
            s = _sig(raw_ref[...].astype(F32))
            f = lbd + (1.0 - lbd) * s
            df = dg_ref[:, :HW] / f - dk_ref[:, :HW].astype(F32)
            put(2 + d, df * (1.0 - lbd) * s * (1.0 - s))
            dlb_ref[d:d + 1, :] += _colsum(df * (1.0 - s)) * (lbd * (1.0 - lbd))
            dxg.append(dg_ref[:, HW:] * (1.0 / GLA_NORM) * _sig(-xg[:, d * HW:(d + 1) * HW]))
        dxg = jnp.concatenate(dxg, axis=1)
        db_ref[0:1, :] += _colsum(dxg)
        dxg_b = dxg.astype(BF16)
        dw_ref[...] += _dot(lr, dxg_b, TN)
        dlr = _dot(dxg_b, wgk_ref[...], NT)
        dp_ref[:, LR0:LR0 + DH] = (dlr + dgm_ref[:, :DH].astype(F32)).astype(BF16)
        dp_ref[:, LR0 + DH:GATE_GLA0] = dgm_ref[:, DH:D]
        dp_ref[:, GATE_GLA0:GATE_GLA0 + DH] = dgm_ref[:, D:GW] + dgm_ref[:, GW:GW + DH]
        dp_ref[:, GATE_GLA0 + DH:GATE_GLA0 + GW] = dgm_ref[:, GW + DH:]
        dp_ref[:, GATE_GLA0 + GW:] = jnp.zeros((TR, W_IN_COLS - GATE_GLA0 - GW), BF16)

    return pl.pallas_call(
        body, name="gates_bwd", grid=(t // TR,),
        in_specs=[seg(0), seg(2), seg(3), _rowcol(DH, LR0 // DH), _full((2, 2, HW)), _full((DH, D)), _full((1, D)),
                  _row(2 * GW), _row(D)] + [_row(D)] * 8,
        out_specs=[_row(W_IN_COLS), _full((8, HW)), _full((DH, D)), _full((8, D))],
        out_shape=[jax.ShapeDtypeStruct((t, W_IN_COLS), BF16), jax.ShapeDtypeStruct((8, HW), F32),
                   jax.ShapeDtypeStruct((DH, D), F32), jax.ShapeDtypeStruct((8, D), F32)],
        compiler_params=_cp("arbitrary"),
    )(p, p, p, p, hg_lb, wgk, bgk, dgm, dgo, dq_f, dq_b, dv_f, dv_b, dk_f, dk_b, dg_f, dg_b)


def _scan_consts(rev):
    r = lax.broadcasted_iota(jnp.int32, (CHUNK, CHUNK), 0)
    u = lax.broadcasted_iota(jnp.int32, (CHUNK, CHUNK), 1)
    rp = lax.broadcasted_iota(jnp.int32, (CHUNK, 1), 0)
    if rev:
        r, u, rp = CHUNK - 1 - r, CHUNK - 1 - u, CHUNK - 1 - rp
    tri = jnp.where(u <= r, 1.0, 0.0).astype(F32)
    tri_t = jnp.where(r <= u, 1.0, 0.0).astype(F32)
    lv = []
    for b in LEVELS:
        sh = b.bit_length() - 1
        pair = ((r >> sh) == (u >> sh) + 1) & (((u >> sh) & 1) == 0)
        pair_t = ((u >> sh) == (r >> sh) + 1) & (((r >> sh) & 1) == 0)
        tside = ((rp >> sh) & 1) == 1
        lv.append((pair, pair_t, tside, jnp.where(tside, 1.0, -1.0).astype(F32)))
    bd = LEVELS[-1].bit_length() - 1
    diag = ((r >> bd) == (u >> bd)) & (u <= r)
    diag_t = ((r >> bd) == (u >> bd)) & (r <= u)
    return tri, tri_t, lv, diag, diag_t


def _row_of(pos, rev):
    return CHUNK - 1 - pos if rev else pos


def _chunk_terms(cum, b_scr, consts, rev):
    _, _, lv, _, _ = consts
    terms = []
    for b, (_, _, _, sgn) in zip(LEVELS, lv):
        pieces = []
        for j in range(CHUNK // (2 * b)):
            row = _row_of(2 * b * j + b - 1, rev)
            pieces.append(jnp.broadcast_to(b_scr[row:row + 1, :], (2 * b, DH)))
        if rev:
            pieces = pieces[::-1]
        bnd = pieces[0] if len(pieces) == 1 else jnp.concatenate(pieces, axis=0)
        terms.append(jnp.exp((cum - bnd) * sgn))
    b = LEVELS[-1]
    pieces = []
    for j in range(CHUNK // b):
        if j == 0:
            pieces.append(jnp.zeros((b, DH), F32))
        else:
            row = _row_of(b * j - 1, rev)
            pieces.append(jnp.broadcast_to(b_scr[row:row + 1, :], (b, DH)))
    if rev:
        pieces = pieces[::-1]
    start = jnp.concatenate(pieces, axis=0)
    wq = jnp.exp(jnp.minimum(cum - start, 0.0))
    wk = jnp.exp(jnp.minimum(start - cum, EXP_CLAMP))
    terms.append((wq, wk))
    return terms


def _run_staged(units):
    live = list(units)
    while live:
        nxt = []
        for u in live:
            try:
                next(u)
                nxt.append(u)
            except StopIteration:
                pass
        live = nxt


SCAN_TB = 256
SCAN_CB = SCAN_TB // CHUNK


def _block_order(i, ntb, rev):
    nctx = CTX // SCAN_TB
    if not rev:
        return i
    return jnp.where(i < nctx, nctx - 1 - i, ntb - 1 - (i - nctx))


def _chunk_in_block(j, rev):
    return SCAN_CB - 1 - j if rev else j


def _scan_fwd(q, k, v, g, rev):
    t = q.shape[0]
    nc = t // CHUNK
    hpb = SCAN_HEADS_FWD

    def body(q_ref, k_ref, v_ref, g_ref, o_ref, st_ref, s_scr, b_scr):
        consts = _scan_consts(rev)
        _, _, lv, diag, _ = consts
        masks = [lvl[0] for lvl in lv] + [diag]

        @pl.when(pl.program_id(1) == 0)
        def _():
            s_scr[...] = jnp.zeros_like(s_scr)

        tri = consts[0]
        state = {hh: s_scr[hh] for hh in range(hpb)}

        def unit(hh, j):
            sl = slice(hh * DH, (hh + 1) * DH)
            c = _chunk_in_block(j, rev)
            rows = slice(c * CHUNK, (c + 1) * CHUNK)
            b_ref = b_scr.at[hh * SCAN_CB + j]
            qc, kc, vc, gc = q_ref[rows, sl], k_ref[rows, sl], v_ref[rows, sl], g_ref[rows, sl]
            cum = _split_dot(tri, gc)
            b_ref[...] = cum
            yield
            terms = _chunk_terms(cum, b_ref, consts, rev)
            qf, kf = qc.astype(F32), kc.astype(F32)
            xs = [(jnp.where(tside, qf, kf) * w).astype(BF16) for w, (_, _, tside, _) in zip(terms[:-1], lv)]
            qd, kd = (qf * terms[-1][0]).astype(BF16), (kf * terms[-1][1]).astype(BF16)
            tot = _colsum(gc)
            qe = (qf * jnp.exp(cum)).astype(BF16)
            ke = (kf * jnp.exp(tot - cum)).astype(BF16)
            vb = vc.astype(BF16)
            yield
            scs = [_dot(x, x, NT) for x in xs] + [_dot(qd, kd, NT)]
            kv = _dot(vb, ke, TN)
            yield
            a = jnp.zeros((CHUNK, CHUNK), F32)
            for sc, m in zip(scs, masks):
                a = a + jnp.where(m, sc, 0.0)
            o_intra = _dot(a.astype(BF16), vb, NN)
            yield
            st = state[hh]
            st_ref[hh, c] = st
            o_ref[rows, sl] = o_intra + _dot(qe, st.astype(BF16), NT)
            state[hh] = st * jnp.exp(tot) + kv
            yield

        _run_staged([unit(hh, j) for hh in range(hpb) for j in range(SCAN_CB)])
        for hh in range(hpb):
            s_scr[hh] = state[hh]

    ntb = t // SCAN_TB
    col = pl.BlockSpec((SCAN_TB, hpb * DH), lambda h, i: (_block_order(i, ntb, rev), h))
    return pl.pallas_call(
        body, name="scan_fwd_" + ("bw" if rev else "fw"), grid=(NH // hpb, ntb),
        in_specs=[col] * 4,
        out_specs=[col, pl.BlockSpec((hpb, SCAN_CB, DH, DH), lambda h, i: (h, _block_order(i, ntb, rev), 0, 0))],
        out_shape=[jax.ShapeDtypeStruct((t, D), F32), jax.ShapeDtypeStruct((NH, nc, DH, DH), F32)],
        scratch_shapes=[pltpu.VMEM((hpb, DH, DH), F32), pltpu.VMEM((hpb * SCAN_CB, CHUNK, DH), F32)],
        compiler_params=_cp("parallel", "arbitrary"),
    )(q, k, v, g)


def _scan_bwd(q, k, v, g, do, states, rev):
    t = q.shape[0]
    nc = t // CHUNK
    hpb = SCAN_HEADS_BWD

    def body(q_ref, k_ref, v_ref, g_ref, do_ref, st_ref, dq_ref, dk_ref, dv_ref, dg_ref, ds_scr, b_scr):
        consts = _scan_consts(rev)
        _, tri_t, lv, diag, diag_t = consts
        masks = [(lvl[0], lvl[1]) for lvl in lv] + [(diag, diag_t)]
        @pl.when(pl.program_id(1) == 0)
        def _():
            ds_scr[...] = jnp.zeros_like(ds_scr)

        tri = consts[0]
        dstate = {hh: ds_scr[hh] for hh in range(hpb)}

        def unit(hh, jj):
            sl = slice(hh * DH, (hh + 1) * DH)
            c = _chunk_in_block(SCAN_CB - 1 - jj, rev)
            rows = slice(c * CHUNK, (c + 1) * CHUNK)
            b_ref = b_scr.at[hh * SCAN_CB + jj]
            qc, kc, vc, gc = q_ref[rows, sl], k_ref[rows, sl], v_ref[rows, sl], g_ref[rows, sl]
            dob = do_ref[rows, sl].astype(BF16)
            vb = vc.astype(BF16)
            cum = _split_dot(tri, gc)
            b_ref[...] = cum
            da = _dot(dob, vb, NT)
            da_t = _dot(vb, dob, NT)
            yield
            terms = _chunk_terms(cum, b_ref, consts, rev)
            qf, kf = qc.astype(F32), kc.astype(F32)
            xs = [(jnp.where(tside, qf, kf) * w).astype(BF16) for w, (_, _, tside, _) in zip(terms[:-1], lv)]
            wqd, wkd = terms[-1]
            qdb, kdb = (qf * wqd).astype(BF16), (kf * wkd).astype(BF16)
            tot = _colsum(gc)
            e_tot = jnp.exp(tot)
            e_b = jnp.exp(cum)
            e_t = jnp.exp(tot - cum)
            qeb = (qf * e_b).astype(BF16)
            keb = (kf * e_t).astype(BF16)
            dsym = [(jnp.where(m, da, 0.0) + jnp.where(m_t, da_t, 0.0)).astype(BF16) for m, m_t in masks[:-1]]
            dad = (jnp.where(diag, da, 0.0).astype(BF16), jnp.where(diag_t, da_t, 0.0).astype(BF16))
            yield
            sym = [_dot(x, x, NT) for x in xs]
            dxs = [_dot(d, x, NN) for d, x in zip(dsym, xs)]
            at_d = _dot(kdb, qdb, NT)
            dqt_d = _dot(dad[0], kdb, NN)
            dkt_d = _dot(dad[1], qdb, NN)
            qd = _dot(dob, qeb, TN)
            yield
            a_t = jnp.where(diag_t, at_d, 0.0)
            dq = dqt_d * wqd
            dk = dkt_d * wkd
            db = dqt_d * qdb.astype(F32) - dkt_d * kdb.astype(F32)
            for s, dx, x, w, (_, m_t, tside, sgn) in zip(sym, dxs, xs, terms[:-1], lv):
                a_t = a_t + jnp.where(m_t, s, 0.0)
                dxw = dx * w
                dq = dq + jnp.where(tside, dxw, 0.0)
                dk = dk + jnp.where(tside, 0.0, dxw)
                db = db + (dx * x.astype(F32)) * sgn
            dv_intra = _dot(a_t.astype(BF16), dob, NN)
            st = st_ref[hh, c]
            stb = st.astype(BF16)
            dqe = _dot(dob, stb, NN)
            yield
            dst = dstate[hh]
            dstb = dst.astype(BF16)
            dstate[hh] = dst * e_tot + qd
            dv_ref[rows, sl] = (dv_intra + _dot(keb, dstb, NT)).astype(BF16)
            dke = _dot(vb, dstb, NN)
            yield
            qe = qeb.astype(F32)
            ke = keb.astype(F32)
            dq_ref[rows, sl] = (dq + dqe * e_b).astype(BF16)
            dk_ref[rows, sl] = (dk + dke * e_t).astype(BF16)
            db = db + dqe * qe - dke * ke
            dtot = _colsum(dstb.astype(F32) * stb.astype(F32)) * e_tot + _colsum(dke * ke)
            dg_ref[rows, sl] = _split_dot(tri_t, db) + dtot
            yield

        _run_staged([unit(hh, jj) for hh in range(hpb) for jj in range(SCAN_CB)])
        for hh in range(hpb):
            ds_scr[hh] = dstate[hh]

    ntb = t // SCAN_TB
    blk = lambda i: _block_order(ntb - 1 - i, ntb, rev)
    col = pl.BlockSpec((SCAN_TB, hpb * DH), lambda h, i: (blk(i), h))
    out = jax.ShapeDtypeStruct((t, D), F32)
    outb = jax.ShapeDtypeStruct((t, D), BF16)
    return pl.pallas_call(
        body, name="scan_bwd_" + ("bw" if rev else "fw"), grid=(NH // hpb, ntb),
        in_specs=[col] * 5 + [pl.BlockSpec((hpb, SCAN_CB, DH, DH), lambda h, i: (h, blk(i), 0, 0))],
        out_specs=[col] * 4,
        out_shape=[outb] * 3 + [out],
        scratch_shapes=[pltpu.VMEM((hpb, DH, DH), F32), pltpu.VMEM((hpb * SCAN_CB, CHUNK, DH), F32)],
        compiler_params=_cp("parallel", "arbitrary"),
    )(q, k, v, g, do, states)


W_IN_GRAD_CHUNKS = (("a", (0, 512)), ("b", (0, 256)), ("b", (256, 512)))
W_IN_REF = 6688
W_IN_PAD = 896
W_IN_PIECE = 256
W_IN_STAGES = (2, 4)


def _assemble_w_in(g, rows, prev, name):
    n, r, wp = g.shape
    tr = W_IN_PIECE
    tiles = wp // DH
    first = rows[0] // tr

    def body(g_ref, *refs):
        o_ref = refs[-1]
        lane = lax.broadcasted_iota(jnp.int32, (tr, DH), 1)
        for t in range(W_IN_COLS // DH):
            acc = None
            for j in range(n):
                c = DH * t - W_IN_SHARD * j
                if c <= -DH or c >= W_IN_SHARD:
                    continue
                k, s = divmod(c, DH)
                lo = g_ref[j, :, k * DH:(k + 1) * DH] if 0 <= k < tiles else None
                hi = g_ref[j, :, (k + 1) * DH:(k + 2) * DH] if s and 0 <= k + 1 < tiles else None
                if s:
                    zero = jnp.zeros((tr, DH), g.dtype)
                    lo = zero if lo is None else pltpu.roll(lo, DH - s, 1)
                    hi = zero if hi is None else pltpu.roll(hi, DH - s, 1)
                    part = jnp.where(lane < DH - s, lo, hi)
                else:
                    part = lo
                acc = part if acc is None else acc + part
            o_ref[:, t * DH:(t + 1) * DH] = jnp.zeros((tr, DH), g.dtype) if acc is None else acc

    held = [] if prev is None else [prev]
    return pl.pallas_call(
        body, name=name, grid=((rows[1] - rows[0]) // tr,),
        in_specs=[pl.BlockSpec((n, tr, wp), lambda i: (0, first + i, 0))] + [pl.BlockSpec(memory_space=pl.ANY)] * len(held),
        out_specs=pl.BlockSpec((tr, W_IN_COLS), lambda i: (first + i, 0)),
        out_shape=jax.ShapeDtypeStruct((r, W_IN_COLS), g.dtype),
        input_output_aliases={1: 0} if held else {},
        compiler_params=_cp("parallel"),
    )(g, *held)


def _gate_cols(w):
    return jnp.pad(w, ((0, 0), (GOFF, GW - GOFF - D)))


def _gate_rows(w):
    return jnp.pad(w, ((GOFF, GW - GOFF - D), (0, 0)))


def _layout_wgk(w):
    r = w.shape[1]
    top = jnp.concatenate([w[0], jnp.zeros_like(w[0])], axis=1)
    bot = jnp.concatenate([jnp.zeros_like(w[1]), w[1]], axis=1)
    return jnp.concatenate([top, bot, jnp.zeros((DH - 2 * r, D), w.dtype)], axis=0)


def _unlayout_wgk(d, r=16):
    return jnp.stack([d[:r, :HW], d[r:2 * r, HW:]])


def _local_step(z, target, modc, modx, norms, onw, hg_lb, wgk, bgk, get_w_in, get_mix, get_ffn, send):
    n_pre1, n_post1, n_pre2, n_post2 = norms
    t = z[0].shape[0] + z[1].shape[0]
    tm = 1152 if t % 1152 == 0 else 256
    h1 = _prenorm(z, n_pre1, modc, modx, 0, 1, "prenorm1")
    w_in = get_w_in(h1)
    p = _matmul(h1, w_in, NN, BF16, "mm_in", t, 1024, D)
    q, v, k_f, k_b, g_f, g_b = _gates_fwd(p, hg_lb, wgk, bgk)
    o_f, st_f = _scan_fwd(q, k_f, v, g_f, False)
    o_b, st_b = _scan_fwd(q, k_b, v, g_b, True)
    y = _post_fwd(o_f, o_b, p, onw)
    w_br_hg, w_br_gla, w_out = get_mix(y)
    u1, u2, merged = _branch_merge(y, w_br_hg, w_br_gla, p)
    y1 = _matmul(merged, w_out, NN, BF16, "mm_out", tm, 512, GW)
    z1, h2 = _mid_fwd(z, y1, n_post1, n_pre2, modc, modx)
    w_gu_t, get_down = get_ffn(h2)
    u, v_ff, act = _mm_gu_act(h2, w_gu_t[0], w_gu_t[1], "mm_gu", tm)
    w_down = get_down(act)
    y2 =_matmul(act, w_down, NN, BF16, "mm_down", t, 512, D_FF)
    dz, dy2, loss_vec, sm_final = _final(z1, y2, target, n_post2, modc, modx)
    du, dv_ff = _mm_down_dx_act(dy2, w_down, u, v_ff, "mm_down_dx", tm)
    d_w_down = _matmul(act, dy2, TN, BF16, "mm_down_dw", D_FF // 2, 1024, t)
    dh2 = _matmul((du, dv_ff), w_gu_t, NN, BF16, "mm_gu_dx", tm, 512, D_FF)
    d_w_gate_t = _matmul(du, h2, TN, BF16, "mm_gate_dw", D_FF // 2, 1024, t)
    d_w_up_t = _matmul(dv_ff, h2, TN, BF16, "mm_up_dw", D_FF // 2, 1024, t)
    dh2 = send(("w_down", "w_gate_t", "w_up_t"), (d_w_down, d_w_gate_t, d_w_up_t), dh2)
    dz, dy1, sm_mid = _mid_bwd(dh2, dz, z1, y1, n_post1, n_pre2, modc, modx)
    dmerged = _matmul(dy1, w_out, NT, BF16, "mm_out_dx", tm, GW, D)
    d_w_out = _matmul(merged, dy1, TN, BF16, "mm_out_dw", GW, 512, t)
    du1, du2, dgm, dy_hg, dy_gla = _branch_merge_bwd(dmerged, p, u1, u2, w_br_hg, w_br_gla)
    d_w_br_hg = _matmul(y, du1, TN, BF16, "mm_br_hg_dw", HW, GW, t, a_off=0, m_out=HW)
    d_w_br_gla = _matmul(y, du2, TN, BF16, "mm_br_gla_dw", HW, GW, t, a_off=1, m_out=HW)
    dy_hg = send(("w_out", "w_br_hg", "w_br_gla"), (d_w_out, d_w_br_hg, d_w_br_gla), dy_hg)
    do, dgo, sm_post = _post_bwd(dy_hg, dy_gla, o_f, o_b, p, onw)
    dq_f, dk_f, dv_f, dg_f = _scan_bwd(q, k_f, v, g_f, do, st_f, False)
    dq_b, dk_b, dv_b, dg_b = _scan_bwd(q, k_b, v, g_b, do, st_b, True)
    dp, d_lb, d_wgk, d_bgk = _gates_bwd(p, hg_lb, wgk, bgk, dgm, dgo, dq_f, dq_b, dv_f, dv_b, dk_f, dk_b, dg_f, dg_b)
    d_w_in_a = _matmul(h1, dp, TN, BF16, "mm_in_dw_a", 512, 1024, t, a_off=0, m_out=D // 2)
    dp = send(("w_in_a",), (d_w_in_a,), dp)
    d_w_in_b = _matmul(h1, dp, TN, BF16, "mm_in_dw_b", 512, 1024, t, a_off=1, m_out=D // 2)
    dp = send(("w_in_b",), (d_w_in_b,), dp)
    dh1 = _matmul(dp, w_in, NT, BF16, "mm_in_dx", tm, 512, W_IN_COLS // 2)
    grad_x, sm_pre = _pre_bwd(dh1, dz, z, n_pre1, modc, modx)
    return dict(loss_vec=loss_vec, grad_x=grad_x, sm_final=sm_final, sm_mid=sm_mid, sm_post=sm_post, sm_pre=sm_pre,
                d_lb=d_lb, d_wgk=d_wgk, d_bgk=d_bgk)


MESH = pl.DeviceIdType.MESH
ANY = pl.BlockSpec(memory_space=pl.ANY)
N_REL = N_DEV - 1


def _place():
    return lax.axis_index("x"), lax.axis_index("y"), lax.axis_index("c")


def _slot(p):
    return 4 * p[0] + 2 * p[1] + p[2]


HBM = pl.BlockSpec(memory_space=pltpu.HBM)
SEM = pl.BlockSpec(memory_space=pltpu.SEMAPHORE)
EFFECT = pltpu.SideEffectType.DATAFLOW_SIDE_EFFECTING


def _peer_of(x, y, c, k):
    flip = lambda v, bit: 1 - v if bit else v
    return flip(x, k & 4), flip(y, k & 2), flip(c, k & 1)


def _view_whole(src, slot):
    return src


def _view_near(src, slot):
    return src


_view_near.peers = (1, 2, 4)


def _view_near_rows(rows):
    def view(src, slot):
        return src.at[pl.ds(rows[0], rows[1] - rows[0])]
    view.peers = _view_near.peers
    view.land = lambda land, slot: land.at[slot, pl.ds(rows[0], rows[1] - rows[0])]
    return view


def _view_block(src, slot):
    return src.at[slot]


def _view_cols(src, slot):
    return src.at[:, pl.ds(pl.multiple_of(slot * (D // N_DEV), D // N_DEV), D // N_DEV)]


W_IN_SHARD = W_IN_REF // N_DEV


def _view_window(rows):
    def view(src, slot):
        col0 = pl.multiple_of((W_IN_SHARD * slot // DH) * DH, DH)
        return src.at[pl.ds(rows[0], rows[1] - rows[0]), pl.ds(col0, D)]
    return view


def _split_copies(view, srcs, lands, send_sems, recv_sems, local_sems):
    x, y, c = _place()
    me = _slot((x, y, c))
    into = getattr(view, "land", lambda land, slot: land.at[slot])
    local, sends, waits = [], [], []
    for a, (src, land) in enumerate(zip(srcs, lands)):
        local.append(pltpu.make_async_copy(view(src, me), into(land, me), local_sems.at[a]))
        for k in getattr(view, "peers", range(1, N_DEV)):
            peer = _peer_of(x, y, c, k)
            mine = view(src, _slot(peer))
            sems = dict(send_sem=send_sems.at[N_REL * a + k - 1], recv_sem=recv_sems.at[N_REL * a + k - 1],
                        device_id=peer, device_id_type=MESH)
            sends.append(pltpu.make_async_remote_copy(src_ref=mine, dst_ref=into(land, me), **sems))
            waits.append(pltpu.make_async_remote_copy(src_ref=mine, dst_ref=into(land, _slot(peer)), **sems))
    return local, sends, waits


def _split_start(groups, name, after):
    built = []
    for view, srcs, lands in groups:
        lands = [lax.empty(l, s.dtype) if isinstance(l, tuple) else l for l, s in zip(lands, srcs)]
        built.append((view, list(srcs), lands))
    bufs = [b for _, srcs, lands in built for b in srcs + lands]
    nb, ng = len(bufs), len(built)

    def body(*refs):
        buf_refs, sem_refs, token = refs[:nb], refs[nb + 1:nb + 1 + 3 * ng], refs[-1]
        pos = 0
        for i, (view, srcs, _) in enumerate(built):
            n = len(srcs)
            local, sends, _ = _split_copies(view, buf_refs[pos:pos + n], buf_refs[pos + n:pos + 2 * n],
                                            *sem_refs[3 * i:3 * i + 3])
            pos += 2 * n
            for cp in local + sends:
                cp.start()
        token[...] = jnp.zeros_like(token)

    sems = []
    for _, srcs, _ in built:
        n = len(srcs)
        sems += [pltpu.SemaphoreType.DMA((N_REL * n,)), pltpu.SemaphoreType.DMA((N_REL * n,)),
                 pltpu.SemaphoreType.DMA((n,))]
    hbm = lambda a: pltpu.with_memory_space_constraint(a, pltpu.HBM)
    out = pl.pallas_call(
        body, name=name,
        out_shape=(*sems, *[pltpu.HBM(b.shape, b.dtype) for b in bufs], jax.ShapeDtypeStruct((8, DH), F32)),
        in_specs=[HBM] * nb + [ANY],
        out_specs=(*([SEM] * (3 * ng)), *([HBM] * nb), pl.BlockSpec(memory_space=pltpu.VMEM)),
        input_output_aliases={i: 3 * ng + i for i in range(nb)},
        compiler_params=pltpu.CompilerParams(has_side_effects=EFFECT),
    )(*[hbm(b) for b in bufs], after)
    handles, pos = [], 3 * ng
    for i, (view, srcs, _) in enumerate(built):
        n = len(srcs)
        handles.append(dict(view=view, n=n, sems=out[3 * i:3 * i + 3], srcs=list(out[pos:pos + n]),
                            lands=list(out[pos + n:pos + 2 * n])))
        pos += 2 * n
    return handles, out[-1]


def _split_wait(handle, name, after, srcs=None, lands=None):
    view, n, sems = handle["view"], handle["n"], handle["sems"]
    srcs = handle["srcs"] if srcs is None else srcs
    lands = handle["lands"] if lands is None else lands
    afters = list(after) if isinstance(after, (list, tuple)) else [after]

    def body(*refs):
        src_refs, land_refs = refs[:n], refs[n:2 * n]
        send_sems, recv_sems, local_sems = refs[2 * n:2 * n + 3]
        local, _, waits = _split_copies(view, src_refs, land_refs, send_sems, recv_sems, local_sems)
        for cp in waits:
            cp.wait_send()
            cp.wait_recv()
        for cp in local:
            cp.wait()

    out = pl.pallas_call(
        body, name=name,
        out_shape=(*[pltpu.HBM(s.shape, s.dtype) for s in srcs], *[pltpu.HBM(l.shape, l.dtype) for l in lands]),
        in_specs=[HBM] * (2 * n) + [SEM, SEM, SEM] + [ANY] * len(afters),
        out_specs=tuple([HBM] * (2 * n)),
        input_output_aliases={i: i for i in range(2 * n)},
        compiler_params=pltpu.CompilerParams(has_side_effects=EFFECT),
    )(*srcs, *lands, *sems, *afters)
    handle["srcs"] = list(out[:n])
    return list(out[n:])


def _tie(x, token, name):
    def body(x_ref, t_ref, o_ref):
        pass

    return pl.pallas_call(
        body, name=name, out_shape=jax.ShapeDtypeStruct(x.shape, x.dtype),
        in_specs=[ANY, ANY], out_specs=ANY, input_output_aliases={0: 0},
    )(x, token)


def _forward_diagonal(land, name, rows):
    mid = (rows[0] + rows[1]) // 2
    half_a, half_b = pl.ds(rows[0], mid - rows[0]), pl.ds(mid, rows[1] - mid)

    def body(land_ref, out_ref, send_sems, recv_sems):
        x, y, c = _place()
        diag = _slot((1 - x, 1 - y, c))

        def copy(blk, part, j, to):
            return pltpu.make_async_remote_copy(src_ref=land_ref.at[blk, part], dst_ref=out_ref.at[blk, part],
                                                send_sem=send_sems.at[j], recv_sem=recv_sems.at[j],
                                                device_id=to, device_id_type=MESH)

        sends = [copy(_slot((1 - x, y, c)), half_a, 0, (x, 1 - y, c)),
                 copy(_slot((x, 1 - y, c)), half_b, 1, (1 - x, y, c))]
        for cp in sends:
            cp.start()
        copy(diag, half_a, 0, (x, 1 - y, c)).wait_recv()
        copy(diag, half_b, 1, (1 - x, y, c)).wait_recv()
        for cp in sends:
            cp.wait_send()

    return pl.pallas_call(
        body, name=name, in_specs=[ANY], out_specs=ANY, input_output_aliases={0: 0},
        out_shape=jax.ShapeDtypeStruct(land.shape, land.dtype),
        scratch_shapes=[pltpu.SemaphoreType.DMA((2,)), pltpu.SemaphoreType.DMA((2,))],
    )(land)


def _forward_to_sibling(land, name, rows):
    def body(land_ref, out_ref, send_sems, recv_sems):
        x, y, c = _place()
        sibling = (x, y, 1 - c)
        chips = [(1 - x, y), (x, 1 - y), (1 - x, 1 - y)]
        piece = pl.ds(rows[0], rows[1] - rows[0])

        def copy(j, core):
            blk = _slot((*chips[j], core))
            return pltpu.make_async_remote_copy(src_ref=land_ref.at[blk, piece], dst_ref=out_ref.at[blk, piece],
                                                send_sem=send_sems.at[j], recv_sem=recv_sems.at[j],
                                                device_id=sibling, device_id_type=MESH)

        sends = [copy(j, c) for j in range(3)]
        for cp in sends:
            cp.start()
        for j in range(3):
            copy(j, 1 - c).wait_recv()
        for cp in sends:
            cp.wait_send()

    return pl.pallas_call(
        body, name=name, in_specs=[ANY], out_specs=ANY, input_output_aliases={0: 0},
        out_shape=jax.ShapeDtypeStruct(land.shape, land.dtype),
        scratch_shapes=[pltpu.SemaphoreType.DMA((3,)), pltpu.SemaphoreType.DMA((3,))],
    )(land)


def _mod_fwd(a, w, b):
    def body(a_ref, w_ref, b_ref, o_ref):
        o_ref[...] = _dot(_silu(a_ref[...]), w_ref[...], NN, precision=HI) + b_ref[...]

    return pl.pallas_call(
        body, name="mod_fwd", out_shape=jax.ShapeDtypeStruct((a.shape[0], w.shape[1]), F32),
        compiler_params=pltpu.CompilerParams(vmem_limit_bytes=VMEM_LIMIT),
    )(a, w, b)


def _mod_bwd(a, d, w):
    def body(a_ref, d_ref, w_ref, dw_ref, dc_ref):
        av = a_ref[...]
        dv = d_ref[...]
        dw_ref[...] = _dot(_silu(av), dv, TN, precision=HI)
        da = _dot(dv[0:8, :], w_ref[...], NT, precision=HI) * _dsilu(av[0:8, :])
        row = lax.broadcasted_iota(jnp.int32, da.shape, 0)
        dc_ref[...] = jnp.where(row == 0, da, 0.0)

    return pl.pallas_call(
        body, name="mod_bwd",
        out_shape=[jax.ShapeDtypeStruct(w.shape, F32), jax.ShapeDtypeStruct((8, w.shape[0]), F32)],
        compiler_params=pltpu.CompilerParams(vmem_limit_bytes=VMEM_LIMIT),
    )(a, d, w)


def _sum_devices(g):
    def body(g_ref, o_ref):
        acc = g_ref[0]
        for i in range(1, g.shape[0]):
            acc = acc + g_ref[i]
        o_ref[...] = acc

    return pl.pallas_call(body, name="sum_devices_%d" % g.shape[1],
                          out_shape=jax.ShapeDtypeStruct(g.shape[1:], F32))(g)


def _sum_windows(g, name):
    n, r, c = g.shape
    tr = 128

    def body(g_ref, o_ref):
        x, y, cc = _place()
        lane0 = (W_IN_SHARD * _slot((x, y, cc))) % DH
        acc = g_ref[0].astype(F32)
        for i in range(1, n):
            acc = acc + g_ref[i].astype(F32)
        o_ref[...] = pltpu.roll(acc, (c - lane0) % c, 1).T

    return pl.pallas_call(
        body, name=name, grid=(r // tr,),
        in_specs=[pl.BlockSpec((n, tr, c), lambda i: (0, i, 0))],
        out_specs=pl.BlockSpec((c, tr), lambda i: (0, i)),
        out_shape=jax.ShapeDtypeStruct((c, r), F32),
        compiler_params=_cp("parallel"),
    )(g)


def _adam_rows(r, c, n):
    budget = 10 * 1024 * 1024
    best = None
    for tr in range(16, r + 1, 16):
        if r % tr == 0 and tr * c * (2 * n + 28) <= budget:
            best = tr
    return best if best is not None else r


def _adamw(g, w, m, v, name):
    n, r, c = g.shape
    tr = _adam_rows(r, c, n)
    bc1 = 1.0 - ADAM_B1 ** ADAM_STEP
    bc2 = 1.0 - ADAM_B2 ** ADAM_STEP

    def body(g_ref, w_ref, m_ref, v_ref, go_ref, d_ref, mo_ref, vo_ref):
        grad = g_ref[0].astype(F32)
        for i in range(1, n):
            grad = grad + g_ref[i].astype(F32)
        go_ref[...] = grad
        m_new = ADAM_B1 * m_ref[...] + (1.0 - ADAM_B1) * grad
        v_new = ADAM_B2 * v_ref[...] + (1.0 - ADAM_B2) * (grad * grad)
        mo_ref[...] = m_new
        vo_ref[...] = v_new
        d_ref[...] = -ADAM_LR * ((m_new / bc1) / (jnp.sqrt(v_new / bc2) + ADAM_EPS) + ADAM_WD * w_ref[...])

    blk = pl.BlockSpec((tr, c), lambda i: (i, 0))
    out = jax.ShapeDtypeStruct((r, c), F32)
    return pl.pallas_call(
        body, name=name, grid=(r // tr,),
        in_specs=[pl.BlockSpec((n, tr, c), lambda i: (0, i, 0)), blk, blk, blk],
        out_specs=[blk] * 4, out_shape=[out] * 4,
        compiler_params=_cp("parallel"),
    )(g, w, m, v)


ADAM_ROWS3 = 168


def _adam_math(grad, w, m, v):
    bc1 = 1.0 - ADAM_B1 ** ADAM_STEP
    bc2 = 1.0 - ADAM_B2 ** ADAM_STEP
    m_new = ADAM_B1 * m + (1.0 - ADAM_B1) * grad
    v_new = ADAM_B2 * v + (1.0 - ADAM_B2) * (grad * grad)
    delta = -ADAM_LR * ((m_new / bc1) / (jnp.sqrt(v_new / bc2) + ADAM_EPS) + ADAM_WD * w)
    return delta, m_new, v_new


def _adamw_rows3(g, w3, m3, v3, name, cols, prev):
    r, _, _ = w3.shape
    c = cols[1] - cols[0]
    n = min(-(-r // 16) * 8, ADAM_ROWS3 * D // c // 8 * 8)
    starts = list(range(0, r - n, n)) + [r - n]
    held = [] if prev is None else list(prev)

    def body(g_hbm, w_hbm, m_hbm, v_hbm, *refs):
        go_hbm, d_hbm, mo_hbm, vo_hbm, gbuf, ibuf, obuf, in_sems, out_sems = refs[len(held):]
        part = lambda h, r0: h.at[pl.ds(r0, n), 0, pl.ds(cols[0], c)]

        def fetch(p):
            r0, slot = starts[p], p % 2
            g0 = (r0 // 8) * 8
            cps = [pltpu.make_async_copy(g_hbm.at[pl.ds(g0, n + 8)], gbuf.at[slot], in_sems.at[slot, 0])]
            cps += [pltpu.make_async_copy(part(h, r0), ibuf.at[slot, k], in_sems.at[slot, 1 + k])
                    for k, h in enumerate((w_hbm, m_hbm, v_hbm))]
            for cp in cps:
                cp.start()
            return cps

        pending, outs = fetch(0), []
        for p, r0 in enumerate(starts):
            slot = p % 2
            nxt = fetch(p + 1) if p + 1 < len(starts) else []
            for cp in pending:
                cp.wait()
            grad = gbuf[slot, pl.ds(r0 - (r0 // 8) * 8, n), :]
            delta, m_new, v_new = _adam_math(grad, ibuf[slot, 0], ibuf[slot, 1], ibuf[slot, 2])
            for cp in outs:
                cp.wait()
            for k, val in enumerate((grad, delta, m_new, v_new)):
                obuf[slot, k] = val
            outs = [pltpu.make_async_copy(obuf.at[slot, k], part(h, r0), out_sems.at[slot, k])
                    for k, h in enumerate((go_hbm, d_hbm, mo_hbm, vo_hbm))]
            for cp in outs:
                cp.start()
            pending = nxt
        for cp in outs:
            cp.wait()

    out = jax.ShapeDtypeStruct(w3.shape, F32)
    return pl.pallas_call(
        body, name=name, in_specs=[ANY] * (4 + len(held)), out_specs=[ANY] * 4, out_shape=[out] * 4,
        input_output_aliases={4 + k: k for k in range(len(held))},
        scratch_shapes=[pltpu.VMEM((2, n + 8, c), F32), pltpu.VMEM((2, 3, n, c), F32), pltpu.VMEM((2, 4, n, c), F32),
                        pltpu.SemaphoreType.DMA((2, 4)), pltpu.SemaphoreType.DMA((2, 4))],
        compiler_params=pltpu.CompilerParams(vmem_limit_bytes=VMEM_LIMIT),
    )(g, w3, m3, v3, *held)


def kernel(x, c, ctx, c_ctx, w_mod, b_mod, norm_pre1, norm_post1, norm_pre2, norm_post2, w_in, hg_lb, hg_onorm, gla_w_gk, gla_b_gk, gla_onorm, w_br_hg, w_br_gla, w_out, w_ff_gate, w_ff_up, w_ff_down, loss_target, m_c_ctx, m_w_mod, m_b_mod, m_norm_pre1, m_norm_post1, m_norm_pre2, m_norm_post2, m_w_in, m_hg_lb, m_hg_onorm, m_gla_w_gk, m_gla_b_gk, m_gla_onorm, m_w_br_hg, m_w_br_gla, m_w_out, m_w_ff_gate, m_w_ff_up, m_w_ff_down, v_c_ctx, v_w_mod, v_b_mod, v_norm_pre1, v_norm_post1, v_norm_pre2, v_norm_post2, v_w_in, v_hg_lb, v_hg_onorm, v_gla_w_gk, v_gla_b_gk, v_gla_onorm, v_w_br_hg, v_w_br_gla, v_w_out, v_w_ff_gate, v_w_ff_up, v_w_ff_down):
    xi, yi, ci = lax.axis_index("x"), lax.axis_index("y"), lax.axis_index("c")
    me = 4 * xi + 2 * yi + ci
    t = CTX + x.shape[1]

    w_in_pieces, w_in_state = [], {}

    def w_in_piece(i):
        return (_view_near_rows((i * W_IN_PIECE, (i + 1) * W_IN_PIECE)), w_in_state["src"], w_in_state["land"])

    def started_w_in(handle):
        w_in_state.update(src=handle["srcs"], land=handle["lands"])
        w_in_pieces.append(handle)

    tr_ = lambda a: jnp.swapaxes(a[0], 0, 1)
    w_in_bf = jnp.pad(w_in[0].astype(BF16), ((0, 0), (0, W_IN_PAD - W_IN_SHARD)))
    w_in_state.update(src=[w_in_bf], land=[lax.empty((N_DEV,) + w_in_bf.shape, BF16)])
    gathered = lambda arrs: [(N_DEV,) + a.shape for a in arrs]
    whole = lambda arrs: (_view_whole, arrs, gathered(arrs))
    small_in = [c, hg_lb, gla_w_gk[0], gla_b_gk[0]]
    (small_handle, piece), tok = _split_start([whole(small_in), w_in_piece(0)], "ag_small_start", c)
    started_w_in(piece)
    c_all, lb_g, wgk_g, bgk_g = _split_wait(small_handle, "ag_small_wait", tok)
    big = [w_in[0], w_br_hg[0], w_br_gla[0], w_out[0], tr_(w_ff_gate), tr_(w_ff_up), w_ff_down[0]]
    big_bf = [None] + [w.astype(BF16) for w in big[1:]]
    cols = lambda g: jnp.transpose(g, (1, 0, 2)).reshape(g.shape[1], N_DEV * g.shape[2])

    def get_w_in(after):
        w_full, first = None, 0
        for s, last in enumerate(W_IN_STAGES):
            for i in range(first, last):
                land = _split_wait(w_in_pieces[i], "ag_w_in_wait%d" % i, after if w_full is None else [after, w_full],
                                   srcs=w_in_state["src"], lands=w_in_state["land"])
                w_in_state.update(src=w_in_pieces[i]["srcs"], land=land)
            rows = (first * W_IN_PIECE, last * W_IN_PIECE)
            crossed = _forward_diagonal(w_in_state["land"][0], "ag_w_in_diagonal%d" % s, rows)
            w_in_state["land"] = [_forward_to_sibling(crossed, "ag_w_in_forward%d" % s, rows)]
            w_full = _assemble_w_in(w_in_state["land"][0], rows, w_full, "assemble_w_in%d" % s)
            first = last
        handles, tok = _split_start([whole(big_bf[1:4]), whole(big_bf[4:6]), whole(big_bf[6:])], "ag_big_start",
                                    w_full)
        big_handles.update(mix=handles[0], ffn=handles[1], down=handles[2])
        return _tie(w_full, tok, "tie_big")

    big_handles = {}

    def get_mix(after):
        g_brh, g_brg, g_out = _split_wait(big_handles["mix"], "ag_mix_wait", after)
        return _gate_cols(cols(g_brh)), _gate_cols(cols(g_brg)), _gate_rows(g_out.reshape(D, D))

    def get_ffn(after):
        g_gate, g_up = _split_wait(big_handles["ffn"], "ag_ffn_wait", after)

        def get_down(after):
            g_down, = _split_wait(big_handles["down"], "ag_down_wait", after)
            return g_down.reshape(D_FF, D)

        return (g_gate.reshape(D_FF, D), g_up.reshape(D_FF, D)), get_down

    hg_lb_full = jnp.transpose(lb_g, (1, 2, 0, 3)).reshape(2, 2, HW)
    wgk_k = _layout_wgk(jnp.transpose(wgk_g, (1, 2, 0, 3)).reshape(2, 16, HW)).astype(BF16)
    bgk_k = jnp.transpose(bgk_g, (1, 0, 2)).reshape(1, D)
    onw = jnp.concatenate([jnp.tile(hg_onorm, (1, NH // 2)), jnp.tile(gla_onorm, (1, NH // 2))], axis=1)

    n_mod = w_mod.shape[2]
    a9 = jnp.concatenate([c_ctx[None], c_all[:, 0], jnp.zeros((16 - 1 - N_DEV, D), F32)], axis=0)
    b_loc = lax.dynamic_slice(b_mod, (0, me * n_mod), (1, n_mod))
    s_loc = _mod_fwd(a9, w_mod[0], b_loc)
    (mod_handle, piece), tok = _split_start([whole([s_loc]), w_in_piece(1)], "ag_mod_start", s_loc)
    started_w_in(piece)
    for i in range(2, D // W_IN_PIECE):
        (piece,), tok = _split_start([w_in_piece(i)], "ag_w_in_start%d" % i, tok)
        started_w_in(piece)
    s_all, = _split_wait(mod_handle, "ag_mod_wait", tok)
    mod_all = jnp.transpose(s_all, (1, 0, 2)).reshape(16, N_DEV * n_mod)
    pad8 = lambda m: jnp.concatenate([m.reshape(6, D), jnp.zeros((2, D), F32)], axis=0)
    modc = pad8(mod_all[0])
    modx = pad8(lax.dynamic_slice(mod_all, (1 + me, 0), (1, N_DEV * n_mod))[0])

    z = (ctx[0], x[0])
    modx = _tie(modx, tok, "tie_mod")
    norms = (norm_pre1, norm_post1, norm_pre2, norm_post2)
    rowshard = lambda d: d.reshape(N_DEV, d.shape[0] // N_DEV, d.shape[1]).astype(BF16)
    sent, w_in_grad = [], {}

    def w_in_chunk(i):
        half, rows = W_IN_GRAD_CHUNKS[i]
        return (_view_window(rows), w_in_grad[half], [(N_DEV, rows[1] - rows[0], D)])

    def sent_w_in(i, handle):
        w_in_grad[W_IN_GRAD_CHUNKS[i][0]] = handle["srcs"]
        sent.append(("w_in%d" % i, ["w_in#%d" % i], handle))

    def send(names, grads, x_after):
        if names == ("w_in_a",):
            w_in_grad["a"] = list(grads)
            (handle,), tok = _split_start([w_in_chunk(0)], "grads_w_in0_start", x_after)
            sent_w_in(0, handle)
            return _tie(x_after, tok, "tie_w_in0")
        if names == ("w_in_b",):
            w_in_grad["b"] = list(grads)
            return x_after
        arrs, leaves, col_arrs, col_leaves = [], [], [], []
        for nm, g in zip(names, grads):
            if nm in ("w_gate_t", "w_up_t"):
                arrs.append(rowshard(g))
                leaves.append({"w_gate_t": "w_ff_gate", "w_up_t": "w_ff_up"}[nm])
            elif nm == "w_down":
                arrs.append(rowshard(g))
                leaves.append("w_ff_down")
            elif nm == "w_out":
                arrs.append(rowshard(g[GOFF:GOFF + D]))
                leaves.append(nm)
            else:
                col_arrs.append(g[:, GOFF:GOFF + D])
                col_leaves.append(nm)
        groups = [(_view_block, arrs, [a.shape for a in arrs])]
        if col_arrs:
            groups.append((_view_cols, col_arrs, [(N_DEV, a.shape[0], D // N_DEV) for a in col_arrs]))
        handles, tok = _split_start(groups, "grads_%s_start" % names[0], x_after)
        sent.append((names[0], leaves, handles[0]))
        if col_arrs:
            sent.append((names[0] + "_cols", col_leaves, handles[1]))
        return _tie(x_after, tok, "tie_" + names[0])

    r = _local_step(z, loss_target[0], modc, modx, norms, onw, hg_lb_full, wgk_k, bgk_k,
                    get_w_in, get_mix, get_ffn, send)
    grad_x = r["grad_x"][None]

    sm_pre, sm_mid, sm_fin = r["sm_pre"], r["sm_mid"], r["sm_final"]
    dmodc = jnp.stack([sm_pre[0], sm_pre[2], sm_mid[4], sm_mid[0], sm_mid[2], sm_fin[0]]).reshape(-1)
    dmodx = jnp.stack([sm_pre[1], sm_pre[3], sm_mid[5], sm_mid[1], sm_mid[3], sm_fin[1]]).reshape(-1)
    on = r["sm_post"][0].reshape(NH, DH)
    pieces = [dmodc, dmodx, sm_pre[4], sm_mid[7], sm_mid[6], sm_fin[2], on[:NH // 2].sum(0), on[NH // 2:].sum(0),
              r["d_lb"][:2].reshape(-1), _unlayout_wgk(r["d_wgk"]).reshape(-1), r["d_bgk"][0]]
    loss_local = (0.5 / D) * jnp.sum(r["loss_vec"])
    pieces.append(jnp.concatenate([loss_local.reshape(1), jnp.zeros((DH - 1,), F32)]))
    sizes = [p.shape[0] for p in pieces]
    pack = jnp.concatenate(pieces).reshape(-1, DH)
    moms = [(m_w_in, v_w_in), (m_w_br_hg, v_w_br_hg), (m_w_br_gla, v_w_br_gla), (m_w_out, v_w_out),
            (m_w_ff_gate, v_w_ff_gate), (m_w_ff_up, v_w_ff_up), (m_w_ff_down, v_w_ff_down)]
    names = ["w_in", "w_br_hg", "w_br_gla", "w_out", "w_ff_gate", "w_ff_up", "w_ff_down"]
    wmv = {nm: (w, m, v) for nm, w, (m, v) in zip(names, big, moms)}
    res, updated = {}, {}

    def update(nm):
        w, m, v = wmv[nm]
        if nm in ("w_ff_gate", "w_ff_up"):
            outs = _adamw(recv[nm], w, tr_(m), tr_(v), "adamw_" + nm)
            res[nm] = [jnp.swapaxes(o, 0, 1)[None] for o in outs]
        else:
            outs = _adamw(recv[nm], w, m[0], v[0], "adamw_" + nm)
            res[nm] = [o[None] for o in outs]
        updated[nm] = outs[0]

    (small_handle, handle), tok = _split_start([whole([pack]), w_in_chunk(1)], "small_grads_start", pack)
    sent_w_in(1, handle)
    recv = {}
    for first, leaves, handle in sent:
        if not first.startswith("w_in"):
            recv.update(zip(leaves, _split_wait(handle, "grads_%s_wait" % first, tok)))
    update("w_ff_gate")
    update("w_ff_up")
    pack_all, = _split_wait(small_handle, "small_grads_wait", [updated["w_ff_gate"], updated["w_ff_up"]])
    tot = _sum_devices(pack_all).reshape(-1)
    offs = [sum(sizes[:i]) for i in range(len(sizes))]
    part = lambda i: tot[offs[i]:offs[i] + sizes[i]]
    dmodc_t, dmodx_t = part(0), part(1)
    g_b_mod = (dmodc_t + dmodx_t)[None]
    g_norms = [part(i)[None] for i in (2, 3, 4, 5)]
    g_hg_on, g_gla_on = part(6)[None], part(7)[None]
    lb0 = lax.dynamic_slice(part(8).reshape(2, HW), (0, me * (HW // N_DEV)), (2, HW // N_DEV))
    g_hg_lb = jnp.stack([lb0, -lb0])
    g_wgk = lax.dynamic_slice(part(9).reshape(2, 16, HW), (0, 0, me * (HW // N_DEV)), (2, 16, HW // N_DEV))[None]
    g_bgk = lax.dynamic_slice(part(10).reshape(2, HW), (0, me * (HW // N_DEV)), (2, HW // N_DEV))[None]
    loss = part(11)[0]

    dmx_all = pack_all.reshape(N_DEV, -1)[:, sizes[0]:sizes[0] + sizes[1]]
    d9 = jnp.concatenate([lax.dynamic_slice(dmodc_t[None], (0, me * n_mod), (1, n_mod)),
                          lax.dynamic_slice(dmx_all, (0, me * n_mod), (N_DEV, n_mod)),
                          jnp.zeros((16 - 1 - N_DEV, n_mod), F32)], axis=0)
    g_w_mod, dcc_part = _mod_bwd(a9, d9, w_mod[0])
    (cctx_handle, handle), tok = _split_start([whole([dcc_part]), w_in_chunk(2)], "c_ctx_start", dcc_part)
    sent_w_in(2, handle)
    recv["w_ff_down"] = _tie(recv["w_ff_down"], tok, "tie_down")
    update("w_ff_down")
    res["w_mod"] = [o[None] for o in _adamw(g_w_mod[None], w_mod[0], m_w_mod[0], v_w_mod[0], "adamw_w_mod")]
    for nm in ("w_out", "w_br_hg", "w_br_gla"):
        update(nm)
    dcc_all, = _split_wait(cctx_handle, "c_ctx_wait", [updated[nm] for nm in names[1:]] + [res["w_mod"][0]])
    g_c_ctx = _sum_devices(dcc_all)[0]

    small = [("c_ctx", c_ctx, m_c_ctx, v_c_ctx, g_c_ctx), ("b_mod", b_mod, m_b_mod, v_b_mod, g_b_mod),
             ("norm_pre1", norm_pre1, m_norm_pre1, v_norm_pre1, g_norms[0]),
             ("norm_post1", norm_post1, m_norm_post1, v_norm_post1, g_norms[1]),
             ("norm_pre2", norm_pre2, m_norm_pre2, v_norm_pre2, g_norms[2]),
             ("norm_post2", norm_post2, m_norm_post2, v_norm_post2, g_norms[3]),
             ("hg_lb", hg_lb, m_hg_lb, v_hg_lb, g_hg_lb), ("hg_onorm", hg_onorm, m_hg_onorm, v_hg_onorm, g_hg_on),
             ("gla_w_gk", gla_w_gk, m_gla_w_gk, v_gla_w_gk, g_wgk), ("gla_b_gk", gla_b_gk, m_gla_b_gk, v_gla_b_gk, g_bgk),
             ("gla_onorm", gla_onorm, m_gla_onorm, v_gla_onorm, g_gla_on)]
    flat = lambda k: jnp.concatenate([s[k].reshape(-1) for s in small]).reshape(-1, DH)
    outs = _adamw(flat(4)[None], flat(1), flat(2), flat(3), "adamw_small")
    off = 0
    for nm, w, _, _, _ in small:
        res[nm] = [o.reshape(-1)[off:off + w.size].reshape(w.shape) for o in outs]
        off += w.size

    done = [updated[nm] for nm in names[1:]] + [res["w_mod"][0]] + [o for nm, *_ in small for o in res[nm]]
    major = lambda a: jnp.transpose(a, (2, 0, 1))
    outs, row0 = None, 0
    for i, (first, leaves, handle) in enumerate(s for s in sent if s[0].startswith("w_in")):
        half = W_IN_GRAD_CHUNKS[i][0]
        land, = _split_wait(handle, "grads_%s_wait" % first, done, srcs=w_in_grad[half])
        w_in_grad[half] = handle["srcs"]
        rows = (row0, row0 + land.shape[1])
        outs = _adamw_rows3(_sum_windows(land, "sum_windows%d" % i), major(w_in), major(m_w_in), major(v_w_in),
                            "adamw_w_in%d" % i, rows, outs)
        row0 = rows[1]
    res["w_in"] = [jnp.transpose(o, (1, 2, 0)) for o in outs]

    order = ["c_ctx", "w_mod", "b_mod", "norm_pre1", "norm_post1", "norm_pre2", "norm_post2", "w_in", "hg_lb",
             "hg_onorm", "gla_w_gk", "gla_b_gk", "gla_onorm", "w_br_hg", "w_br_gla", "w_out", "w_ff_gate", "w_ff_up",
             "w_ff_down"]
    return (loss, grad_x, *[res[n][k] for k in range(4) for n in order])
```
